```python
import jax, jax.numpy as jnp
from jax import lax
import numpy as np


D_MODEL = 1024
BATCH = 8
SEQ = 2048
DEPTH = 1

CHUNK = 64
Q_BLOCK = 128
D_A = D_MODEL // 2
HEAD_A = 64
H_A = D_A // HEAD_A
DECAY_RANK = 64
AAA_RANK = 64
D_B = D_MODEL // 2
HEAD_B = 64
H_B = D_B // HEAD_B
RWKV_COLS = 4 * D_A + DECAY_RANK + AAA_RANK
FOX_COLS = 4 * D_B + H_B
GATE_COLS = 2 * D_MODEL
IN_COLS = RWKV_COLS + FOX_COLS + GATE_COLS
RMS_EPS = 1e-6
LNX_EPS = 64e-5

kernel_name = 'hybrid_rwkv7_fox_gated_block'


def _rmsnorm(x, g):
    xf = x.astype(jnp.float32)
    y = xf * lax.rsqrt(jnp.mean(xf * xf, axis=-1, keepdims=True) + RMS_EPS)
    return (y * g.astype(jnp.float32)).astype(x.dtype)


def _token_shift(u):
    return jnp.pad(u, ((0, 0), (1, 0), (0, 0)))[:, :-1]


def _wkv7_scan(r, decay, k, v, a_vec, b_vec):
    B, S, H, N = r.shape

    def step(state, inp):
        r_t, w_t, k_t, v_t, a_t, b_t = inp
        sa = jnp.einsum('bhij,bhj->bhi', state, a_t)
        state = (state * w_t[:, :, None, :]
                 + sa[..., None] * b_t[:, :, None, :]
                 + v_t[..., None] * k_t[:, :, None, :])
        return state, jnp.einsum('bhij,bhj->bhi', state, r_t)

    xs = (jnp.moveaxis(r, 1, 0), jnp.moveaxis(decay, 1, 0), jnp.moveaxis(k, 1, 0),
          jnp.moveaxis(v, 1, 0), jnp.moveaxis(a_vec, 1, 0), jnp.moveaxis(b_vec, 1, 0))
    state0 = jnp.zeros((B, H, N, N), jnp.float32)
    _, y = lax.scan(step, state0, xs)
    return jnp.moveaxis(y, 0, 1)


def _rwkv7_mixer(u, mu, w_up, w0, a_up, a0, k_k, k_a, r_k, lnx_w, lnx_b):
    B, S, _ = u.shape
    u = u + (_token_shift(u) - u) * mu
    r, k, v, wd, ad, gate = jnp.split(
        u, [D_A, 2 * D_A, 3 * D_A, 3 * D_A + DECAY_RANK, 3 * D_A + DECAY_RANK + AAA_RANK], axis=-1)
    w = -jax.nn.softplus(-(w0 + jnp.tanh(wd) @ w_up)) - 0.5
    decay = jnp.exp(-jnp.exp(w.astype(jnp.float32)))
    a = jax.nn.sigmoid(a0 + ad @ a_up)
    heads = lambda t: t.reshape(B, S, H_A, HEAD_A).astype(jnp.float32)
    kk = heads(k * k_k)
    kk = kk / jnp.maximum(jnp.linalg.norm(kk, axis=-1, keepdims=True), 1e-12)
    k = k * (1.0 + (a - 1.0) * k_a)
    r_h, k_h, v_h, a_h = heads(r), heads(k), heads(v), heads(a)
    y = _wkv7_scan(r_h, heads(decay), k_h, v_h, -kk, kk * a_h)
    mean = jnp.mean(y, axis=-1, keepdims=True)
    var = jnp.mean(jnp.square(y - mean), axis=-1, keepdims=True)
    y = (y - mean) * lax.rsqrt(var + LNX_EPS)
    y = y * lnx_w.reshape(H_A, HEAD_A).astype(jnp.float32) + lnx_b.reshape(H_A, HEAD_A).astype(jnp.float32)
    bonus = jnp.sum(r_h * k_h * r_k.astype(jnp.float32), axis=-1, keepdims=True) * v_h
    y = (y + bonus).reshape(B, S, D_A).astype(u.dtype)
    return y * jax.nn.silu(gate)


def _fox_mixer(u, f_bias, q_norm_g, k_norm_g):
    B, S, _ = u.shape
    q, k, v, gate, f_logit = jnp.split(u, [D_B, 2 * D_B, 3 * D_B, 4 * D_B], axis=-1)
    to_bhsd = lambda t: jnp.transpose(t, (0, 2, 1, 3))
    q = to_bhsd(_rmsnorm(q.reshape(B, S, H_B, HEAD_B), q_norm_g))
    k = to_bhsd(_rmsnorm(k.reshape(B, S, H_B, HEAD_B), k_norm_g))
    v = to_bhsd(v.reshape(B, S, H_B, HEAD_B))
    log_f = jax.nn.log_sigmoid((f_logit + f_bias).astype(jnp.float32))
    cum = jnp.transpose(jnp.cumsum(log_f, axis=1), (0, 2, 1))
    scale = HEAD_B ** -0.5
    outs = []
    for i in range(S // Q_BLOCK):
        lo, hi = i * Q_BLOCK, (i + 1) * Q_BLOCK
        qb, kp, vp = q[:, :, lo:hi], k[:, :, :hi], v[:, :, :hi]
        logits = (jnp.einsum('bhqd,bhkd->bhqk', qb, kp).astype(jnp.float32) * scale
                  + cum[:, :, lo:hi, None] - cum[:, :, None, :hi])
        causal = (lo + jnp.arange(Q_BLOCK))[:, None] >= jnp.arange(hi)[None, :]
        logits = jnp.where(causal, logits, -jnp.inf)
        p = jax.nn.softmax(logits, axis=-1)
        outs.append(jnp.einsum('bhqk,bhkd->bhqd', p.astype(vp.dtype), vp))
    o = jnp.concatenate(outs, axis=2)
    o = jnp.transpose(o, (0, 2, 1, 3)).reshape(B, S, D_B)
    return o * jax.nn.silu(gate)


def _fwd_setup_inputs(seed: int = 0) -> dict:
    key = jax.random.key(seed)
    ks = jax.random.split(key, 20)
    nrm = lambda k, shape, s: jax.random.normal(k, shape, jnp.float32) * s
    unif = lambda k, shape, lo, hi: jax.random.uniform(k, shape, jnp.float32, lo, hi)
    return {
        'x': jax.random.normal(ks[0], (BATCH, SEQ, D_MODEL), jnp.float32),
        'norm_g': 1.0 + nrm(ks[1], (DEPTH, D_MODEL), 0.02),
        'w_in': nrm(ks[2], (DEPTH, D_MODEL, IN_COLS), D_MODEL ** -0.5),
        'shift_mu': unif(ks[3], (DEPTH, RWKV_COLS), 0.0, 1.0),
        'w_lora_up': nrm(ks[4], (DEPTH, DECAY_RANK, D_A), 0.1 * DECAY_RANK ** -0.5),
        'w0': unif(ks[5], (DEPTH, D_A), -6.0, -1.0),
        'a_lora_up': nrm(ks[6], (DEPTH, AAA_RANK, D_A), 0.1 * AAA_RANK ** -0.5),
        'a0': nrm(ks[7], (DEPTH, D_A), 0.5),
        'k_k': 0.85 + nrm(ks[8], (DEPTH, D_A), 0.02),
        'k_a': 1.0 + nrm(ks[9], (DEPTH, D_A), 0.02),
        'r_k': nrm(ks[10], (DEPTH, H_A, HEAD_A), 0.1),
        'lnx_w': 1.0 + nrm(ks[11], (DEPTH, D_A), 0.02),
        'lnx_b': nrm(ks[12], (DEPTH, D_A), 0.02),
        'f_bias': unif(ks[13], (DEPTH, H_B), 1.0, 5.0),
        'q_norm_g': 1.0 + nrm(ks[14], (DEPTH, HEAD_B), 0.02),
        'k_norm_g': 1.0 + nrm(ks[15], (DEPTH, HEAD_B), 0.02),
        'w_out_a': nrm(ks[16], (DEPTH, D_A, D_MODEL), D_A ** -0.5),
        'w_out_b': nrm(ks[17], (DEPTH, D_B, D_MODEL), D_B ** -0.5),
        'w_out': nrm(ks[18], (DEPTH, D_MODEL, D_MODEL), D_MODEL ** -0.5),
        'final_norm_g': 1.0 + nrm(ks[19], (D_MODEL,), 0.02),
    }


def _fwd_reference(x, norm_g, w_in, shift_mu, w_lora_up, w0, a_lora_up, a0, k_k, k_a, r_k,
              lnx_w, lnx_b, f_bias, q_norm_g, k_norm_g, w_out_a, w_out_b, w_out, final_norm_g):
    for l in range(DEPTH):
        h = _rmsnorm(x, norm_g[l])
        u = h @ w_in[l]
        u_a, u_b, u_g = jnp.split(u, [RWKV_COLS, RWKV_COLS + FOX_COLS], axis=-1)
        y_a = _rwkv7_mixer(u_a, shift_mu[l], w_lora_up[l], w0[l], a_lora_up[l], a0[l],
                           k_k[l], k_a[l], r_k[l], lnx_w[l], lnx_b[l]) @ w_out_a[l]
        y_b = _fox_mixer(u_b, f_bias[l], q_norm_g[l], k_norm_g[l]) @ w_out_b[l]
        g_a, g_b = jnp.split(u_g, 2, axis=-1)
        merged = jax.nn.sigmoid(g_a) * y_a + jax.nn.sigmoid(g_b) * y_b
        x = x + merged @ w_out[l]
    return _rmsnorm(x, final_norm_g)


import jax as _jax
import jax.numpy as _jnp

TWIN_FORMAT = 'train_step'
FWD_PARAMS = ['x', 'norm_g', 'w_in', 'shift_mu', 'w_lora_up', 'w0', 'a_lora_up', 'a0', 'k_k', 'k_a', 'r_k', 'lnx_w', 'lnx_b', 'f_bias', 'q_norm_g', 'k_norm_g', 'w_out_a', 'w_out_b', 'w_out', 'final_norm_g']
TWIN_WEIGHTS = ['norm_g', 'w_in', 'shift_mu', 'w_lora_up', 'w0', 'a_lora_up', 'a0', 'k_k', 'k_a', 'r_k', 'lnx_w', 'lnx_b', 'f_bias', 'q_norm_g', 'k_norm_g', 'w_out_a', 'w_out_b', 'w_out', 'final_norm_g']
TWIN_DIFF_INPUT = 'x'
TWIN_INPUTS = ['x', 'norm_g', 'w_in', 'shift_mu', 'w_lora_up', 'w0', 'a_lora_up', 'a0', 'k_k', 'k_a', 'r_k', 'lnx_w', 'lnx_b', 'f_bias', 'q_norm_g', 'k_norm_g', 'w_out_a', 'w_out_b', 'w_out', 'final_norm_g', 'loss_target', 'm_norm_g', 'm_w_in', 'm_shift_mu', 'm_w_lora_up', 'm_w0', 'm_a_lora_up', 'm_a0', 'm_k_k', 'm_k_a', 'm_r_k', 'm_lnx_w', 'm_lnx_b', 'm_f_bias', 'm_q_norm_g', 'm_k_norm_g', 'm_w_out_a', 'm_w_out_b', 'm_w_out', 'm_final_norm_g', 'v_norm_g', 'v_w_in', 'v_shift_mu', 'v_w_lora_up', 'v_w0', 'v_a_lora_up', 'v_a0', 'v_k_k', 'v_k_a', 'v_r_k', 'v_lnx_w', 'v_lnx_b', 'v_f_bias', 'v_q_norm_g', 'v_k_norm_g', 'v_w_out_a', 'v_w_out_b', 'v_w_out', 'v_final_norm_g']
TWIN_OUTPUTS = ['loss', 'grad_x', 'grad_norm_g', 'grad_w_in', 'grad_shift_mu', 'grad_w_lora_up', 'grad_w0', 'grad_a_lora_up', 'grad_a0', 'grad_k_k', 'grad_k_a', 'grad_r_k', 'grad_lnx_w', 'grad_lnx_b', 'grad_f_bias', 'grad_q_norm_g', 'grad_k_norm_g', 'grad_w_out_a', 'grad_w_out_b', 'grad_w_out', 'grad_final_norm_g', 'delta_norm_g', 'delta_w_in', 'delta_shift_mu', 'delta_w_lora_up', 'delta_w0', 'delta_a_lora_up', 'delta_a0', 'delta_k_k', 'delta_k_a', 'delta_r_k', 'delta_lnx_w', 'delta_lnx_b', 'delta_f_bias', 'delta_q_norm_g', 'delta_k_norm_g', 'delta_w_out_a', 'delta_w_out_b', 'delta_w_out', 'delta_final_norm_g', 'new_m_norm_g', 'new_m_w_in', 'new_m_shift_mu', 'new_m_w_lora_up', 'new_m_w0', 'new_m_a_lora_up', 'new_m_a0', 'new_m_k_k', 'new_m_k_a', 'new_m_r_k', 'new_m_lnx_w', 'new_m_lnx_b', 'new_m_f_bias', 'new_m_q_norm_g', 'new_m_k_norm_g', 'new_m_w_out_a', 'new_m_w_out_b', 'new_m_w_out', 'new_m_final_norm_g', 'new_v_norm_g', 'new_v_w_in', 'new_v_shift_mu', 'new_v_w_lora_up', 'new_v_w0', 'new_v_a_lora_up', 'new_v_a0', 'new_v_k_k', 'new_v_k_a', 'new_v_r_k', 'new_v_lnx_w', 'new_v_lnx_b', 'new_v_f_bias', 'new_v_q_norm_g', 'new_v_k_norm_g', 'new_v_w_out_a', 'new_v_w_out_b', 'new_v_w_out', 'new_v_final_norm_g']
TWIN_LEAF_KINDS = {'loss': 'loss', 'grad_x': 'grad_x', 'grad_norm_g': 'grad_w', 'grad_w_in': 'grad_w', 'grad_shift_mu': 'grad_w', 'grad_w_lora_up': 'grad_w', 'grad_w0': 'grad_w', 'grad_a_lora_up': 'grad_w', 'grad_a0': 'grad_w', 'grad_k_k': 'grad_w', 'grad_k_a': 'grad_w', 'grad_r_k': 'grad_w', 'grad_lnx_w': 'grad_w', 'grad_lnx_b': 'grad_w', 'grad_f_bias': 'grad_w', 'grad_q_norm_g': 'grad_w', 'grad_k_norm_g': 'grad_w', 'grad_w_out_a': 'grad_w', 'grad_w_out_b': 'grad_w', 'grad_w_out': 'grad_w', 'grad_final_norm_g': 'grad_w', 'delta_norm_g': 'delta_w', 'delta_w_in': 'delta_w', 'delta_shift_mu': 'delta_w', 'delta_w_lora_up': 'delta_w', 'delta_w0': 'delta_w', 'delta_a_lora_up': 'delta_w', 'delta_a0': 'delta_w', 'delta_k_k': 'delta_w', 'delta_k_a': 'delta_w', 'delta_r_k': 'delta_w', 'delta_lnx_w': 'delta_w', 'delta_lnx_b': 'delta_w', 'delta_f_bias': 'delta_w', 'delta_q_norm_g': 'delta_w', 'delta_k_norm_g': 'delta_w', 'delta_w_out_a': 'delta_w', 'delta_w_out_b': 'delta_w', 'delta_w_out': 'delta_w', 'delta_final_norm_g': 'delta_w', 'new_m_norm_g': 'new_m', 'new_m_w_in': 'new_m', 'new_m_shift_mu': 'new_m', 'new_m_w_lora_up': 'new_m', 'new_m_w0': 'new_m', 'new_m_a_lora_up': 'new_m', 'new_m_a0': 'new_m', 'new_m_k_k': 'new_m', 'new_m_k_a': 'new_m', 'new_m_r_k': 'new_m', 'new_m_lnx_w': 'new_m', 'new_m_lnx_b': 'new_m', 'new_m_f_bias': 'new_m', 'new_m_q_norm_g': 'new_m', 'new_m_k_norm_g': 'new_m', 'new_m_w_out_a': 'new_m', 'new_m_w_out_b': 'new_m', 'new_m_w_out': 'new_m', 'new_m_final_norm_g': 'new_m', 'new_v_norm_g': 'new_v', 'new_v_w_in': 'new_v', 'new_v_shift_mu': 'new_v', 'new_v_w_lora_up': 'new_v', 'new_v_w0': 'new_v', 'new_v_a_lora_up': 'new_v', 'new_v_a0': 'new_v', 'new_v_k_k': 'new_v', 'new_v_k_a': 'new_v', 'new_v_r_k': 'new_v', 'new_v_lnx_w': 'new_v', 'new_v_lnx_b': 'new_v', 'new_v_f_bias': 'new_v', 'new_v_q_norm_g': 'new_v', 'new_v_k_norm_g': 'new_v', 'new_v_w_out_a': 'new_v', 'new_v_w_out_b': 'new_v', 'new_v_w_out': 'new_v', 'new_v_final_norm_g': 'new_v'}


def _forward(args):
    return _fwd_reference(*[args[k] for k in FWD_PARAMS])


def _output_shape():
    out = _jax.eval_shape(lambda: _forward(_fwd_setup_inputs(0)))
    return out.shape, out.dtype

N_MICROBATCH = 1
ADAM_LR = 0.001
ADAM_B1 = 0.9
ADAM_B2 = 0.999
ADAM_EPS = 1e-08
ADAM_WD = 0.01
ADAM_STEP = 10
PER_EXAMPLE_BATCH_AXIS = {'x': 0, 'loss_target': 0}
SHARED_INPUTS = []
_WEIGHT_DTYPES = {'norm_g': _jnp.float32, 'w_in': _jnp.float32, 'shift_mu': _jnp.float32, 'w_lora_up': _jnp.float32, 'w0': _jnp.float32, 'a_lora_up': _jnp.float32, 'a0': _jnp.float32, 'k_k': _jnp.float32, 'k_a': _jnp.float32, 'r_k': _jnp.float32, 'lnx_w': _jnp.float32, 'lnx_b': _jnp.float32, 'f_bias': _jnp.float32, 'q_norm_g': _jnp.float32, 'k_norm_g': _jnp.float32, 'w_out_a': _jnp.float32, 'w_out_b': _jnp.float32, 'w_out': _jnp.float32, 'final_norm_g': _jnp.float32}
MOMENT_SCALE = {'norm_g': 7.510364e-02, 'w_in': 2.873863e-02, 'shift_mu': 7.609827e-02, 'w_lora_up': 2.087024e-03, 'w0': 1.613881e-02, 'a_lora_up': 1.744581e-02, 'a0': 2.003040e-02, 'k_k': 5.006206e-02, 'k_a': 5.452429e-02, 'r_k': 1.115010e-01, 'lnx_w': 4.762222e-02, 'lnx_b': 4.783090e-02, 'f_bias': 6.165195e-02, 'q_norm_g': 3.734431e-02, 'k_norm_g': 3.750917e-02, 'w_out_a': 3.193405e-02, 'w_out_b': 1.214899e-02, 'w_out': 3.367338e-02, 'final_norm_g': 1.598422e+01}


def _to_microbatches(a, axis):
    t = _jnp.moveaxis(a, axis, 0)
    t = t.reshape((N_MICROBATCH, t.shape[0] // N_MICROBATCH) + t.shape[1:])
    return _jnp.moveaxis(t, 1, axis + 1)


def setup_inputs(seed: int = 0) -> dict:
    inp = _fwd_setup_inputs(seed)
    key = _jax.random.fold_in(_jax.random.key(seed), 7919)
    shape, _ = _output_shape()
    out = dict(inp)
    out["loss_target"] = _jax.random.normal(_jax.random.fold_in(key, 0), shape, _jnp.float32)
    for i, name in enumerate(TWIN_WEIGHTS):
        w = inp[name].astype(_jnp.float32)
        if MOMENT_SCALE is None:
            s = _jnp.sqrt(_jnp.mean(_jnp.square(w)) + 1e-30)
        else:
            s = MOMENT_SCALE[name]
        km, kv = _jax.random.split(_jax.random.fold_in(key, i + 1))
        out[name] = w
        out["m_" + name] = s * _jax.random.normal(km, w.shape, _jnp.float32)
        out["v_" + name] = (s * s) * _jax.random.uniform(kv, w.shape, _jnp.float32, 0.5, 1.5)
    if N_MICROBATCH > 1:
        for name, axis in PER_EXAMPLE_BATCH_AXIS.items():
            out[name] = _to_microbatches(out[name], axis)
    return {'x': out['x'], 'norm_g': out['norm_g'], 'w_in': out['w_in'], 'shift_mu': out['shift_mu'], 'w_lora_up': out['w_lora_up'], 'w0': out['w0'], 'a_lora_up': out['a_lora_up'], 'a0': out['a0'], 'k_k': out['k_k'], 'k_a': out['k_a'], 'r_k': out['r_k'], 'lnx_w': out['lnx_w'], 'lnx_b': out['lnx_b'], 'f_bias': out['f_bias'], 'q_norm_g': out['q_norm_g'], 'k_norm_g': out['k_norm_g'], 'w_out_a': out['w_out_a'], 'w_out_b': out['w_out_b'], 'w_out': out['w_out'], 'final_norm_g': out['final_norm_g'], 'loss_target': out['loss_target'], 'm_norm_g': out['m_norm_g'], 'm_w_in': out['m_w_in'], 'm_shift_mu': out['m_shift_mu'], 'm_w_lora_up': out['m_w_lora_up'], 'm_w0': out['m_w0'], 'm_a_lora_up': out['m_a_lora_up'], 'm_a0': out['m_a0'], 'm_k_k': out['m_k_k'], 'm_k_a': out['m_k_a'], 'm_r_k': out['m_r_k'], 'm_lnx_w': out['m_lnx_w'], 'm_lnx_b': out['m_lnx_b'], 'm_f_bias': out['m_f_bias'], 'm_q_norm_g': out['m_q_norm_g'], 'm_k_norm_g': out['m_k_norm_g'], 'm_w_out_a': out['m_w_out_a'], 'm_w_out_b': out['m_w_out_b'], 'm_w_out': out['m_w_out'], 'm_final_norm_g': out['m_final_norm_g'], 'v_norm_g': out['v_norm_g'], 'v_w_in': out['v_w_in'], 'v_shift_mu': out['v_shift_mu'], 'v_w_lora_up': out['v_w_lora_up'], 'v_w0': out['v_w0'], 'v_a_lora_up': out['v_a_lora_up'], 'v_a0': out['v_a0'], 'v_k_k': out['v_k_k'], 'v_k_a': out['v_k_a'], 'v_r_k': out['v_r_k'], 'v_lnx_w': out['v_lnx_w'], 'v_lnx_b': out['v_lnx_b'], 'v_f_bias': out['v_f_bias'], 'v_q_norm_g': out['v_q_norm_g'], 'v_k_norm_g': out['v_k_norm_g'], 'v_w_out_a': out['v_w_out_a'], 'v_w_out_b': out['v_w_out_b'], 'v_w_out': out['v_w_out'], 'v_final_norm_g': out['v_final_norm_g']}


def _loss(weights, diff, rest, loss_target):
    with _jax.named_scope("forward"):
        args = {**rest, TWIN_DIFF_INPUT: diff, **{k: w.astype(_WEIGHT_DTYPES[k]) for k, w in weights.items()}}
        y = _forward(args)
    with _jax.named_scope("loss_head"):
        err = _jnp.square(y.astype(_jnp.float32) - loss_target)
        return 0.5 * _jnp.sum(_jnp.mean(err, axis=-1)) if err.ndim else 0.5 * err


def _adamw(w, g, m, v):
    m = ADAM_B1 * m + (1.0 - ADAM_B1) * g
    v = ADAM_B2 * v + (1.0 - ADAM_B2) * _jnp.square(g)
    m_hat = m / (1.0 - ADAM_B1 ** ADAM_STEP)
    v_hat = v / (1.0 - ADAM_B2 ** ADAM_STEP)
    delta = -ADAM_LR * (m_hat / (_jnp.sqrt(v_hat) + ADAM_EPS) + ADAM_WD * w)
    return delta, m, v


def reference(x, norm_g, w_in, shift_mu, w_lora_up, w0, a_lora_up, a0, k_k, k_a, r_k, lnx_w, lnx_b, f_bias, q_norm_g, k_norm_g, w_out_a, w_out_b, w_out, final_norm_g, loss_target, m_norm_g, m_w_in, m_shift_mu, m_w_lora_up, m_w0, m_a_lora_up, m_a0, m_k_k, m_k_a, m_r_k, m_lnx_w, m_lnx_b, m_f_bias, m_q_norm_g, m_k_norm_g, m_w_out_a, m_w_out_b, m_w_out, m_final_norm_g, v_norm_g, v_w_in, v_shift_mu, v_w_lora_up, v_w0, v_a_lora_up, v_a0, v_k_k, v_k_a, v_r_k, v_lnx_w, v_lnx_b, v_f_bias, v_q_norm_g, v_k_norm_g, v_w_out_a, v_w_out_b, v_w_out, v_final_norm_g):
    given = dict(x=x, norm_g=norm_g, w_in=w_in, shift_mu=shift_mu, w_lora_up=w_lora_up, w0=w0, a_lora_up=a_lora_up, a0=a0, k_k=k_k, k_a=k_a, r_k=r_k, lnx_w=lnx_w, lnx_b=lnx_b, f_bias=f_bias, q_norm_g=q_norm_g, k_norm_g=k_norm_g, w_out_a=w_out_a, w_out_b=w_out_b, w_out=w_out, final_norm_g=final_norm_g, loss_target=loss_target, m_norm_g=m_norm_g, m_w_in=m_w_in, m_shift_mu=m_shift_mu, m_w_lora_up=m_w_lora_up, m_w0=m_w0, m_a_lora_up=m_a_lora_up, m_a0=m_a0, m_k_k=m_k_k, m_k_a=m_k_a, m_r_k=m_r_k, m_lnx_w=m_lnx_w, m_lnx_b=m_lnx_b, m_f_bias=m_f_bias, m_q_norm_g=m_q_norm_g, m_k_norm_g=m_k_norm_g, m_w_out_a=m_w_out_a, m_w_out_b=m_w_out_b, m_w_out=m_w_out, m_final_norm_g=m_final_norm_g, v_norm_g=v_norm_g, v_w_in=v_w_in, v_shift_mu=v_shift_mu, v_w_lora_up=v_w_lora_up, v_w0=v_w0, v_a_lora_up=v_a_lora_up, v_a0=v_a0, v_k_k=v_k_k, v_k_a=v_k_a, v_r_k=v_r_k, v_lnx_w=v_lnx_w, v_lnx_b=v_lnx_b, v_f_bias=v_f_bias, v_q_norm_g=v_q_norm_g, v_k_norm_g=v_k_norm_g, v_w_out_a=v_w_out_a, v_w_out_b=v_w_out_b, v_w_out=v_w_out, v_final_norm_g=v_final_norm_g)
    weights = {n: given[n] for n in TWIN_WEIGHTS}
    shared = {n: given[n] for n in SHARED_INPUTS}
    per_example = {n: given[n] for n in ['x']}
    grad_fn = _jax.value_and_grad(_loss, argnums=(0, 1))

    def one_microbatch(ex, loss_target):
        ex = dict(ex)
        diff = ex.pop(TWIN_DIFF_INPUT)
        return grad_fn(weights, diff, {**shared, **ex}, loss_target)

    if N_MICROBATCH == 1:
        loss, (grad_w, grad_x) = one_microbatch(per_example, given["loss_target"])
    else:
        def body(carry, xs):
            loss_sum, grad_sum = carry
            l_k, (gw_k, gx_k) = one_microbatch(xs[0], xs[1])
            with _jax.named_scope("update"):
                return (loss_sum + l_k, _jax.tree.map(_jnp.add, grad_sum, gw_k)), gx_k

        init = (_jnp.zeros((), _jnp.float32), _jax.tree.map(_jnp.zeros_like, weights))
        (loss, grad_w), grad_x = _jax.lax.scan(body, init, (per_example, given["loss_target"]))
    with _jax.named_scope("update"):
        delta_w, new_m, new_v = {}, {}, {}
        for n in TWIN_WEIGHTS:
            delta_w[n], new_m[n], new_v[n] = _adamw(weights[n], grad_w[n], given["m_" + n], given["v_" + n])
    return (loss, grad_x, *[grad_w[n] for n in TWIN_WEIGHTS], *[delta_w[n] for n in TWIN_WEIGHTS],
            *[new_m[n] for n in TWIN_WEIGHTS], *[new_v[n] for n in TWIN_WEIGHTS])
```

```python
import functools
import math

import jax
import jax.numpy as jnp
from jax import lax
from jax.experimental import pallas as pl
from jax.experimental.pallas import tpu as pltpu

F32 = jnp.float32
BF16 = jnp.bfloat16

D_MODEL = 1024
D_HALF = 512
HEAD = 64
N_HEADS = 8
LORA = 64
RWKV_COLS = 2176
FOX_REAL = 2056
SEC = 2176
GATE_COLS = 2048
IN_COLS = 6280
N_CHIPS = 4
SHARD_COLS = IN_COLS // N_CHIPS
RMS_EPS = 1e-6
LNX_EPS = 64e-5
ATT_SCALE = HEAD ** -0.5
NEG = -1e30

ADAM_LR = 0.001
ADAM_B1 = 0.9
ADAM_B2 = 0.999
ADAM_EPS = 1e-08
ADAM_WD = 0.01
ADAM_STEP = 10

LANES = 128
SUBLANES = 8
VMEM_LIMIT = 56 * 1024 * 1024
MESH = pl.DeviceIdType.MESH


def _params(*sem):
    return pltpu.CompilerParams(dimension_semantics=sem if sem else None, vmem_limit_bytes=VMEM_LIMIT)


def _sigmoid(x):
    return 1.0 / (1.0 + jnp.exp(-x))


def _log_sigmoid(x):
    return jnp.minimum(x, 0.0) - jnp.log(1.0 + jnp.exp(-jnp.abs(x)))


def _head_ones():
    r = lax.broadcasted_iota(jnp.int32, (LANES, LANES), 0) >> 6
    c = lax.broadcasted_iota(jnp.int32, (LANES, LANES), 1) >> 6
    return (r == c).astype(BF16)


def _split3(x):
    hi = x.astype(BF16)
    r1 = x - hi.astype(F32)
    mid = r1.astype(BF16)
    lo = (r1 - mid.astype(F32)).astype(BF16)
    return hi, mid, lo


def _exact_dot(x, ones_bf16, ones_first=False):
    out = None
    for piece in _split3(x):
        if ones_first:
            t = jnp.dot(ones_bf16, piece, preferred_element_type=F32)
        else:
            t = jnp.dot(piece, ones_bf16, preferred_element_type=F32)
        out = t if out is None else out + t
    return out


def _head_sum(x, bd):
    n = x.shape[1] // LANES
    parts = [_exact_dot(x[:, i * LANES:(i + 1) * LANES], bd) for i in range(n)]
    return parts[0] if n == 1 else jnp.concatenate(parts, axis=1)


def _dot_nt(a, b):
    return lax.dot_general(a, b, (((1,), (1,)), ((), ())), preferred_element_type=F32)


def _dot_tn(a, b):
    return lax.dot_general(a, b, (((0,), (0,)), ((), ())), preferred_element_type=F32)


def _colsum(x):
    return jnp.sum(x, axis=0, keepdims=True)


def _rmsnorm_in(x, g, tm=512):
    s, d = x.shape

    def body(x_ref, g_ref, h_ref):
        xv = x_ref[...]
        r = lax.rsqrt(jnp.mean(xv * xv, axis=-1, keepdims=True) + RMS_EPS)
        h_ref[...] = (xv * r * g_ref[...]).astype(BF16)

    return pl.pallas_call(
        body, name="rmsnorm_in", grid=(s // tm,),
        in_specs=[pl.BlockSpec((tm, d), lambda i: (i, 0)), pl.BlockSpec((1, d), lambda i: (0, 0))],
        out_specs=pl.BlockSpec((tm, d), lambda i: (i, 0)),
        out_shape=jax.ShapeDtypeStruct((s, d), BF16), compiler_params=_params("parallel"),
    )(x, g)


def _matmul_nn(a, b, name, tm=512):
    m, k = a.shape
    n = b.shape[1]

    def body(a_ref, b_ref, o_ref):
        o_ref[...] = jnp.dot(a_ref[...], b_ref[...], preferred_element_type=F32)

    return pl.pallas_call(
        body, name=name, grid=(m // tm,),
        in_specs=[pl.BlockSpec((tm, k), lambda i: (i, 0)), pl.BlockSpec((k, n), lambda i: (0, 0))],
        out_specs=pl.BlockSpec((tm, n), lambda i: (i, 0)),
        out_shape=jax.ShapeDtypeStruct((m, n), F32), compiler_params=_params("parallel"),
    )(a, b)


def _matmul_tn_acc(at, b, name, tk=512):
    m, k = at.shape
    n = b.shape[1]

    def body(a_ref, b_ref, o_ref):
        j = pl.program_id(0)

        @pl.when(j == 0)
        def _():
            o_ref[...] = jnp.zeros_like(o_ref)

        o_ref[...] += jnp.dot(a_ref[...], b_ref[...].astype(BF16), preferred_element_type=F32)

    return pl.pallas_call(
        body, name=name, grid=(k // tk,),
        in_specs=[pl.BlockSpec((m, tk), lambda j: (0, j)), pl.BlockSpec((tk, n), lambda j: (j, 0))],
        out_specs=pl.BlockSpec((m, n), lambda j: (0, 0)),
        out_shape=jax.ShapeDtypeStruct((m, n), F32), compiler_params=_params("arbitrary"),
    )(at, b)


def _inproj_bwd(du_a, du_b, du_g, w_a, w_b, w_g, x, dx2, g, tm=256):
    s, d = x.shape

    def body(da_ref, db_ref, dg_ref, wa_ref, wb_ref, wg_ref, x_ref, dx2_ref, g_ref, gx_ref, gg_ref):
        i = pl.program_id(0)

        @pl.when(i == 0)
        def _():
            gg_ref[...] = jnp.zeros_like(gg_ref)

        dh = _dot_nt(da_ref[...].astype(BF16), wa_ref[...])
        dh += _dot_nt(db_ref[...].astype(BF16), wb_ref[...])
        dh += _dot_nt(dg_ref[...].astype(BF16), wg_ref[...])
        xv = x_ref[...]
        r = lax.rsqrt(jnp.mean(xv * xv, axis=-1, keepdims=True) + RMS_EPS)
        xh = xv * r
        gg_ref[...] += _colsum(dh * xh)
        dxh = dh * g_ref[...]
        gx_ref[...] = dx2_ref[...] + r * (dxh - xh * jnp.mean(dxh * xh, axis=-1, keepdims=True))

    row = lambda w: pl.BlockSpec((tm, w), lambda i: (i, 0))
    full = lambda a: pl.BlockSpec(a.shape, lambda i: (0, 0))
    return pl.pallas_call(
        body, name="inproj_bwd", grid=(s // tm,),
        in_specs=[row(SEC), row(SEC), row(GATE_COLS), full(w_a), full(w_b), full(w_g), row(d), row(d), full(g)],
        out_specs=[row(d), pl.BlockSpec((1, d), lambda i: (0, 0))],
        out_shape=[jax.ShapeDtypeStruct((s, d), F32), jax.ShapeDtypeStruct((1, d), F32)],
        compiler_params=_params("arbitrary"),
    )(du_a, du_b, du_g, w_a, w_b, w_g, x, dx2, g)


def _rwkv_elementwise(ua, prev_row, first, mu, wl, w0, a0, kkw, kaw, bd):
    tm = ua.shape[0]
    rows = lax.broadcasted_iota(jnp.int32, (tm, 1), 0)
    prev = jnp.where(first, jnp.zeros_like(prev_row), prev_row)
    shifted = jnp.where(rows == 0, prev, pltpu.roll(ua, 1, 0))
    delta = shifted - ua
    us = ua + delta * mu
    r = us[:, 0:512]
    k0 = us[:, 512:1024]
    v = us[:, 1024:1536]
    lo = us[:, 1536:1664]
    gate = us[:, 1664:2176]
    lane = lax.broadcasted_iota(jnp.int32, (1, LANES), 1)
    th = jnp.tanh(lo)
    lin = jnp.where(lane < LORA, th, lo)
    ll = jnp.dot(lin.astype(BF16), wl, preferred_element_type=F32)
    sz = _sigmoid(w0 + ll[:, :512])
    e = sz * math.exp(-0.5)
    dec = jnp.exp(-e)
    a = _sigmoid(a0 + ll[:, 512:])
    kk0 = k0 * kkw
    ss = _head_sum(kk0 * kk0, bd)
    nrm = jnp.maximum(jnp.sqrt(ss), 1e-12)
    kk = kk0 / nrm
    k = k0 * (1.0 + (a - 1.0) * kaw)
    return dict(delta=delta, us=us, r=r, k0=k0, v=v, lo=lo, gate=gate, th=th, lin=lin, sz=sz, e=e, dec=dec,
                a=a, kk0=kk0, ss=ss, nrm=nrm, kk=kk, k=k)


def _rwkv_prep(u_a, mu, wl, w0, a0, kkw, kaw, tm=256):
    s = u_a.shape[0]

    def body(ua_ref, prev_ref, mu_ref, wl_ref, w0_ref, a0_ref, kkw_ref, kaw_ref,
             r_ref, w_ref, k_ref, v_ref, a_ref, b_ref, g_ref):
        i = pl.program_id(0)
        f = _rwkv_elementwise(ua_ref[...], prev_ref[7:8, :], i == 0, mu_ref[...], wl_ref[...], w0_ref[...],
                              a0_ref[...], kkw_ref[...], kaw_ref[...], _head_ones())
        r_ref[...] = f["r"]
        w_ref[...] = f["dec"]
        k_ref[...] = f["k"]
        v_ref[...] = f["v"]
        a_ref[...] = -f["kk"]
        b_ref[...] = f["kk"] * f["a"]
        g_ref[...] = f["gate"]

    vec = lambda w: pl.BlockSpec((1, w), lambda i: (0, 0))
    out = pl.BlockSpec((tm, D_HALF), lambda i: (i, 0))
    return pl.pallas_call(
        body, name="rwkv_prep", grid=(s // tm,),
        in_specs=[pl.BlockSpec((tm, SEC), lambda i: (i, 0)),
                  pl.BlockSpec((8, SEC), lambda i: (jnp.maximum(i * (tm // 8) - 1, 0), 0)),
                  vec(SEC), pl.BlockSpec((LANES, 2 * D_HALF), lambda i: (0, 0)),
                  vec(D_HALF), vec(D_HALF), vec(D_HALF), vec(D_HALF)],
        out_specs=[out] * 7,
        out_shape=[jax.ShapeDtypeStruct((s, D_HALF), F32)] * 7,
        compiler_params=_params("parallel"),
    )(u_a, u_a, mu, wl, w0, a0, kkw, kaw)


SCAN_TB = 64
N_PAIRS = 4


def _pair_sum(x, left):
    s_l = jnp.sum(jnp.where(left, x, 0.0), axis=1, keepdims=True)
    s_r = jnp.sum(jnp.where(left, 0.0, x), axis=1, keepdims=True)
    return jnp.where(left, s_l, s_r)


def _column(tile, onehot, left):
    col = jnp.sum(jnp.where(onehot, tile, 0.0), axis=1, keepdims=True)
    return jnp.where(left, col[0:HEAD], col[HEAD:2 * HEAD])


def _scan_consts():
    lane = lax.broadcasted_iota(jnp.int32, (HEAD, LANES), 1)
    rowi = lax.broadcasted_iota(jnp.int32, (HEAD, LANES), 0)
    left = lane < HEAD
    diag = rowi == (lane & (HEAD - 1))
    lane_t = lax.broadcasted_iota(jnp.int32, (LANES, LANES), 1)
    return left, diag, lane_t


def _wkv_fwd(r, w, k, a, b, v_t):
    s = r.shape[0]
    tb = SCAN_TB
    sub = LANES // tb

    def body(r_ref, w_ref, k_ref, a_ref, b_ref, vt_ref, y_ref, st_ref, state):
        g = pl.program_id(0)

        @pl.when(g == 0)
        def _():
            state[...] = jnp.zeros_like(state)

        left, diag, lane_t = _scan_consts()
        base = (g % sub) * tb
        sub_row = lax.broadcasted_iota(jnp.int32, (SUBLANES, LANES), 0)

        def group(q, carry):
            rows8 = pl.ds(pl.multiple_of(q * SUBLANES, SUBLANES), SUBLANES)
            for p in range(N_PAIRS):
                lanes = pl.ds(p * LANES, LANES)
                a8, w8, b8, k8, r8 = (x[rows8, lanes] for x in (a_ref, w_ref, b_ref, k_ref, r_ref))
                v_tile = vt_ref[p * LANES:(p + 1) * LANES, :]
                y8 = jnp.zeros((SUBLANES, LANES), F32)
                sp = state[p]
                for i in range(SUBLANES):
                    t = q * SUBLANES + i
                    st_ref[t, p] = sp
                    vb = _column(v_tile, lane_t == (base + t), left)
                    sa = _pair_sum(sp * a8[i:i + 1], left)
                    sp = sp * w8[i:i + 1] + sa * b8[i:i + 1] + vb * k8[i:i + 1]
                    yb = _pair_sum(sp * r8[i:i + 1], left)
                    y8 = jnp.where(sub_row == i, _colsum(jnp.where(diag, yb, 0.0)), y8)
                state[p] = sp
                y_ref[rows8, lanes] = y8
            return carry

        lax.fori_loop(0, tb // SUBLANES, group, 0)

    rows = pl.BlockSpec((tb, D_HALF), lambda g: (g, 0))
    return pl.pallas_call(
        body, name="wkv_fwd", grid=(s // tb,),
        in_specs=[rows] * 5 + [pl.BlockSpec((D_HALF, LANES), lambda g: (0, g // sub))],
        out_specs=[rows, pl.BlockSpec((tb, N_PAIRS, HEAD, LANES), lambda g: (g, 0, 0, 0))],
        out_shape=[jax.ShapeDtypeStruct((s, D_HALF), F32),
                   jax.ShapeDtypeStruct((s, N_PAIRS, HEAD, LANES), F32)],
        scratch_shapes=[pltpu.VMEM((N_PAIRS, HEAD, LANES), F32)],
        compiler_params=_params("arbitrary"),
    )(r, w, k, a, b, v_t)


def _wkv_bwd(r, w, k, a, b, v_t, dy_t, st):
    s = r.shape[0]
    tb = SCAN_TB
    sub = LANES // tb
    nb = s // tb

    def body(r_ref, w_ref, k_ref, a_ref, b_ref, vt_ref, dyt_ref, st_ref,
             dr_ref, dw_ref, dk_ref, dv_ref, da_ref, db_ref, dstate):
        g = pl.program_id(0)

        @pl.when(g == 0)
        def _():
            dstate[...] = jnp.zeros_like(dstate)

        left, diag, lane_t = _scan_consts()
        base = ((nb - 1 - g) % sub) * tb

        sub_row = lax.broadcasted_iota(jnp.int32, (SUBLANES, LANES), 0)

        def group(qq, carry):
            q = tb // SUBLANES - 1 - qq
            rows8 = pl.ds(pl.multiple_of(q * SUBLANES, SUBLANES), SUBLANES)
            for p in range(N_PAIRS):
                lanes = pl.ds(p * LANES, LANES)
                rows_p = slice(p * LANES, (p + 1) * LANES)
                a8, w8, b8, k8, r8 = (x[rows8, lanes] for x in (a_ref, w_ref, b_ref, k_ref, r_ref))
                v_tile, dy_tile = vt_ref[rows_p, :], dyt_ref[rows_p, :]
                outs = [jnp.zeros((SUBLANES, LANES), F32) for _ in range(6)]
                dsp = dstate[p]
                for i in reversed(range(SUBLANES)):
                    t = q * SUBLANES + i
                    ar, wr, br, kr, rr = (x[i:i + 1] for x in (a8, w8, b8, k8, r8))
                    onehot = lane_t == (base + t)
                    sp = st_ref[t, p]
                    vb = _column(v_tile, onehot, left)
                    dyb = _column(dy_tile, onehot, left)
                    sa = _pair_sum(sp * ar, left)
                    sn = sp * wr + sa * br + vb * kr
                    ds = dsp + dyb * rr
                    dvb = _pair_sum(ds * kr, left)
                    dsa = _pair_sum(ds * br, left)
                    new = (_colsum(sn * dyb), _colsum(ds * sp), _colsum(ds * vb), _colsum(jnp.where(diag, dvb, 0.0)),
                           _colsum(sp * dsa), _colsum(ds * sa))
                    outs = [jnp.where(sub_row == i, n, o) for n, o in zip(new, outs)]
                    dsp = ds * wr + dsa * ar
                dstate[p] = dsp
                for ref, o in zip((dr_ref, dw_ref, dk_ref, dv_ref, da_ref, db_ref), outs):
                    ref[rows8, lanes] = o
            return carry

        lax.fori_loop(0, tb // SUBLANES, group, 0)

    rows = pl.BlockSpec((tb, D_HALF), lambda g: (nb - 1 - g, 0))
    cols = pl.BlockSpec((D_HALF, LANES), lambda g: (0, (nb - 1 - g) // sub))
    return pl.pallas_call(
        body, name="wkv_bwd", grid=(nb,),
        in_specs=[rows] * 5 + [cols, cols, pl.BlockSpec((tb, N_PAIRS, HEAD, LANES), lambda g: (nb - 1 - g, 0, 0, 0))],
        out_specs=[rows] * 6,
        out_shape=[jax.ShapeDtypeStruct((s, D_HALF), F32)] * 6,
        scratch_shapes=[pltpu.VMEM((N_PAIRS, HEAD, LANES), F32)],
        compiler_params=_params("arbitrary"),
    )(r, w, k, a, b, v_t, dy_t, st)


def _rwkv_post_math(y, r, k, v, gate, lw, lb, rk, bd):
    mean = _head_sum(y, bd) * (1.0 / HEAD)
    yc = y - mean
    var = _head_sum(yc * yc, bd) * (1.0 / HEAD)
    rstd = lax.rsqrt(var + LNX_EPS)
    yn = yc * rstd
    rkk = _head_sum(r * k * rk, bd)
    sg = _sigmoid(gate)
    pre = yn * lw + lb + rkk * v
    return yn, rstd, rkk, sg, pre


def _rwkv_post(y, r, k, v, gate, lw, lb, rk, tm=256):
    s = y.shape[0]

    def body(y_ref, r_ref, k_ref, v_ref, g_ref, lw_ref, lb_ref, rk_ref, o_ref):
        gate_v = g_ref[...]
        _, _, _, sg, pre = _rwkv_post_math(y_ref[...], r_ref[...], k_ref[...], v_ref[...], gate_v,
                                           lw_ref[...], lb_ref[...], rk_ref[...], _head_ones())
        o_ref[...] = pre * (gate_v * sg)

    blk = pl.BlockSpec((tm, D_HALF), lambda i: (i, 0))
    vec = pl.BlockSpec((1, D_HALF), lambda i: (0, 0))
    return pl.pallas_call(
        body, name="rwkv_post", grid=(s // tm,),
        in_specs=[blk] * 5 + [vec] * 3, out_specs=blk,
        out_shape=jax.ShapeDtypeStruct((s, D_HALF), F32), compiler_params=_params("parallel"),
    )(y, r, k, v, gate, lw, lb, rk)


def _rwkv_post_bwd(dmix, y, r, k, v, gate, lw, lb, rk, tm=256):
    s = y.shape[0]

    def body(dm_ref, y_ref, r_ref, k_ref, v_ref, g_ref, lw_ref, lb_ref, rk_ref,
             dy_ref, dr_ref, dk_ref, dv_ref, dg_ref, dlw_ref, dlb_ref, drk_ref):
        i = pl.program_id(0)

        @pl.when(i == 0)
        def _():
            dlw_ref[...] = jnp.zeros_like(dlw_ref)
            dlb_ref[...] = jnp.zeros_like(dlb_ref)
            drk_ref[...] = jnp.zeros_like(drk_ref)

        bd = _head_ones()
        rv, kv, vv, gate_v, lw_v, rk_v = r_ref[...], k_ref[...], v_ref[...], g_ref[...], lw_ref[...], rk_ref[...]
        yn, rstd, rkk, sg, pre = _rwkv_post_math(y_ref[...], rv, kv, vv, gate_v, lw_v, lb_ref[...], rk_v, bd)
        dm = dm_ref[...]
        dg_ref[...] = dm * pre * (sg * (1.0 + gate_v * (1.0 - sg)))
        dpre = dm * (gate_v * sg)
        dlw_ref[...] += _colsum(dpre * yn)
        dlb_ref[...] += _colsum(dpre)
        dyn = dpre * lw_v
        m1 = _head_sum(dyn, bd) * (1.0 / HEAD)
        m2 = _head_sum(dyn * yn, bd) * (1.0 / HEAD)
        dy_ref[...] = rstd * (dyn - m1 - yn * m2)
        dv_ref[...] = dpre * rkk
        drkk = _head_sum(dpre * vv, bd)
        dr_ref[...] = drkk * kv * rk_v
        dk_ref[...] = drkk * rv * rk_v
        drk_ref[...] += _colsum(drkk * rv * kv)

    blk = pl.BlockSpec((tm, D_HALF), lambda i: (i, 0))
    vec = pl.BlockSpec((1, D_HALF), lambda i: (0, 0))
    return pl.pallas_call(
        body, name="rwkv_post_bwd", grid=(s // tm,),
        in_specs=[blk] * 6 + [vec] * 3, out_specs=[blk] * 5 + [vec] * 3,
        out_shape=[jax.ShapeDtypeStruct((s, D_HALF), F32)] * 5 + [jax.ShapeDtypeStruct((1, D_HALF), F32)] * 3,
        compiler_params=_params("arbitrary"),
    )(dmix, y, r, k, v, gate, lw, lb, rk)


def _rwkv_prep_bwd(u_a, grads, mu, wl, w0, a0, kkw, kaw, tm=256):
    s = u_a.shape[0]
    nb = s // tm

    def body(ua_ref, prev_ref, drs_ref, dws_ref, dks_ref, dvs_ref, das_ref, dbs_ref, drb_ref, dkb_ref, dvb_ref,
             dgt_ref, mu_ref, wl_ref, w0_ref, a0_ref, kkw_ref, kaw_ref,
             du_ref, dmu_ref, dwl_ref, dw0_ref, da0_ref, dkkw_ref, dkaw_ref, carry):
        i = pl.program_id(0)

        @pl.when(i == 0)
        def _():
            carry[...] = jnp.zeros_like(carry)
            for ref in (dmu_ref, dwl_ref, dw0_ref, da0_ref, dkkw_ref, dkaw_ref):
                ref[...] = jnp.zeros_like(ref)

        bd = _head_ones()
        mu_v, wl_v, kkw_v, kaw_v = mu_ref[...], wl_ref[...], kkw_ref[...], kaw_ref[...]
        f = _rwkv_elementwise(ua_ref[...], prev_ref[7:8, :], i == nb - 1, mu_v, wl_v, w0_ref[...],
                              a0_ref[...], kkw_v, kaw_v, bd)
        a, kk, k0 = f["a"], f["kk"], f["k0"]
        dk = dks_ref[...] + dkb_ref[...]
        dbs = dbs_ref[...]
        dkk = dbs * a - das_ref[...]
        da = dbs * kk + dk * k0 * kaw_v
        dk0 = dk * (1.0 + (a - 1.0) * kaw_v)
        dkaw_ref[...] += _colsum(dk * k0 * (a - 1.0))
        inv = 1.0 / f["nrm"]
        proj = _head_sum(dkk * kk, bd)
        dkk0 = jnp.where(f["ss"] > 1e-24, (dkk - kk * proj) * inv, dkk * inv)
        dk0 = dk0 + dkk0 * kkw_v
        dkkw_ref[...] += _colsum(dkk0 * k0)
        dza = da * a * (1.0 - a)
        da0_ref[...] += _colsum(dza)
        dz = -dws_ref[...] * f["dec"] * f["e"] * (1.0 - f["sz"])
        dw0_ref[...] += _colsum(dz)
        dll = jnp.concatenate([dz, dza], axis=1).astype(BF16)
        dwl_ref[...] += _dot_tn(f["lin"].astype(BF16), dll)
        dlin = _dot_nt(dll, wl_v)
        lane = lax.broadcasted_iota(jnp.int32, (1, LANES), 1)
        th = f["th"]
        dlo = jnp.where(lane < LORA, dlin * (1.0 - th * th), dlin)
        dus = jnp.concatenate([drs_ref[...] + drb_ref[...], dk0, dvs_ref[...] + dvb_ref[...], dlo, dgt_ref[...]],
                              axis=1)
        dmu_ref[...] += _colsum(dus * f["delta"])
        g1 = dus * mu_v
        rows = lax.broadcasted_iota(jnp.int32, (tm, 1), 0)
        up = jnp.where(rows == tm - 1, carry[...], pltpu.roll(g1, tm - 1, 0))
        du_ref[...] = dus - g1 + up
        carry[...] = g1[0:1, :]

    rev = lambda w: pl.BlockSpec((tm, w), lambda i: (nb - 1 - i, 0))
    vec = lambda w: pl.BlockSpec((1, w), lambda i: (0, 0))
    wl_spec = pl.BlockSpec((LANES, 2 * D_HALF), lambda i: (0, 0))
    return pl.pallas_call(
        body, name="rwkv_prep_bwd", grid=(nb,),
        in_specs=[rev(SEC), pl.BlockSpec((8, SEC), lambda i: (jnp.maximum((nb - 1 - i) * (tm // 8) - 1, 0), 0))]
                 + [rev(D_HALF)] * 10 + [vec(SEC), wl_spec] + [vec(D_HALF)] * 4,
        out_specs=[rev(SEC), vec(SEC), wl_spec] + [vec(D_HALF)] * 4,
        out_shape=[jax.ShapeDtypeStruct((s, SEC), F32), jax.ShapeDtypeStruct((1, SEC), F32),
                   jax.ShapeDtypeStruct((LANES, 2 * D_HALF), F32)] + [jax.ShapeDtypeStruct((1, D_HALF), F32)] * 4,
        scratch_shapes=[pltpu.VMEM((1, SEC), F32)],
        compiler_params=_params("arbitrary"),
    )(u_a, u_a, *grads, mu, wl, w0, a0, kkw, kaw)


def _tri(tm, lower):
    r = lax.broadcasted_iota(jnp.int32, (tm, tm), 0)
    c = lax.broadcasted_iota(jnp.int32, (tm, tm), 1)
    return ((r >= c) if lower else (r <= c)).astype(BF16)


def _head_rms(x, g, bd):
    rinv = lax.rsqrt(_head_sum(x * x, bd) * (1.0 / HEAD) + RMS_EPS)
    xh = x * rinv
    return xh, rinv, xh * g


def _fox_prep(u_b, fb, qg, kg, tm=256):
    s = u_b.shape[0]

    def body(ub_ref, fb_ref, qg_ref, kg_ref, q_ref, k_ref, v_ref, cc_ref, cr_ref, carry):
        i = pl.program_id(0)

        @pl.when(i == 0)
        def _():
            carry[...] = jnp.zeros_like(carry)

        bd = _head_ones()
        _, _, qn = _head_rms(ub_ref[:, 0:512], qg_ref[...], bd)
        _, _, kn = _head_rms(ub_ref[:, 512:1024], kg_ref[...], bd)
        q_ref[...] = (qn * ATT_SCALE).astype(BF16)
        k_ref[...] = kn.astype(BF16)
        v_ref[...] = ub_ref[:, 1024:1536].astype(BF16)
        lane = lax.broadcasted_iota(jnp.int32, (1, LANES), 1)
        logf = jnp.where(lane < N_HEADS, _log_sigmoid(ub_ref[:, 2048:2176] + fb_ref[...]), 0.0)
        cum = _exact_dot(logf, _tri(tm, True), ones_first=True) + carry[...]
        cc_ref[...] = cum
        cr_ref[...] = jnp.transpose(cum)[0:N_HEADS, :]
        carry[...] = cum[tm - 1:tm, :]

    blk = pl.BlockSpec((tm, D_HALF), lambda i: (i, 0))
    return pl.pallas_call(
        body, name="fox_prep", grid=(s // tm,),
        in_specs=[pl.BlockSpec((tm, SEC), lambda i: (i, 0)), pl.BlockSpec((1, LANES), lambda i: (0, 0)),
                  pl.BlockSpec((1, D_HALF), lambda i: (0, 0)), pl.BlockSpec((1, D_HALF), lambda i: (0, 0))],
        out_specs=[blk, blk, blk, pl.BlockSpec((tm, LANES), lambda i: (i, 0)),
                   pl.BlockSpec((N_HEADS, tm), lambda i: (0, i))],
        out_shape=[jax.ShapeDtypeStruct((s, D_HALF), BF16)] * 3
                  + [jax.ShapeDtypeStruct((s, LANES), F32), jax.ShapeDtypeStruct((N_HEADS, s), F32)],
        scratch_shapes=[pltpu.VMEM((1, LANES), F32)],
        compiler_params=_params("arbitrary"),
    )(u_b, fb, qg, kg)


ATT_T = 256


def _attn_fwd(q, k, v, cc, cr, u_b):
    s = q.shape[0]
    t = ATT_T
    nblk = s // t

    def body(q_ref, k_ref, v_ref, cc_ref, cr_ref, g_ref, o_ref, mix_ref, lse_ref, m_sc, l_sc, acc_sc):
        i = pl.program_id(0)
        j = pl.program_id(1)

        @pl.when(j == 0)
        def _():
            m_sc[...] = jnp.full_like(m_sc, NEG)
            l_sc[...] = jnp.zeros_like(l_sc)
            acc_sc[...] = jnp.zeros_like(acc_sc)

        @pl.when(j <= i)
        def _():
            row = i * t + lax.broadcasted_iota(jnp.int32, (t, t), 0)
            col = j * t + lax.broadcasted_iota(jnp.int32, (t, t), 1)
            causal = row >= col
            left = lax.broadcasted_iota(jnp.int32, (1, LANES), 1) < HEAD
            for p in range(N_PAIRS):
                lanes = slice(p * LANES, (p + 1) * LANES)
                q2, k2, v2 = q_ref[:, lanes], k_ref[:, lanes], v_ref[:, lanes]
                acc2 = acc_sc[:, lanes]
                for e in range(2):
                    h = 2 * p + e
                    msk = left if e == 0 else jnp.logical_not(left)
                    sc = _dot_nt(jnp.where(msk, q2, jnp.zeros_like(q2)), k2)
                    sc = sc + (cc_ref[:, h:h + 1] - cr_ref[h:h + 1, :])
                    sc = jnp.where(causal, sc, NEG)
                    m_prev = m_sc[h]
                    m_new = jnp.maximum(m_prev, jnp.max(sc, axis=1, keepdims=True))
                    alpha = jnp.exp(m_prev - m_new)
                    pm = jnp.exp(sc - m_new)
                    l_sc[h] = alpha * l_sc[h] + jnp.sum(pm, axis=1, keepdims=True)
                    m_sc[h] = m_new
                    pv = jnp.dot(pm.astype(BF16), v2, preferred_element_type=F32)
                    acc2 = jnp.where(msk, alpha * acc2 + pv, acc2)
                acc_sc[:, lanes] = acc2

        @pl.when(j == i)
        def _():
            lane = lax.broadcasted_iota(jnp.int32, (1, LANES), 1)
            left = lane < HEAD
            lse = jnp.zeros((t, LANES), F32)
            for p in range(N_PAIRS):
                lanes = slice(p * LANES, (p + 1) * LANES)
                inv = jnp.where(left, 1.0 / l_sc[2 * p], 1.0 / l_sc[2 * p + 1])
                o = acc_sc[:, lanes] * inv
                o_ref[:, lanes] = o
                gate = g_ref[:, lanes]
                mix_ref[:, lanes] = o * (gate * _sigmoid(gate))
                for e in range(2):
                    h = 2 * p + e
                    lse = jnp.where(lane == h, m_sc[h] + jnp.log(l_sc[h]), lse)
            lse_ref[...] = lse

    qblk = pl.BlockSpec((t, D_HALF), lambda i, j: (i, 0))
    kblk = pl.BlockSpec((t, D_HALF), lambda i, j: (jnp.minimum(i, j), 0))
    return pl.pallas_call(
        body, name="fox_attn_fwd", grid=(nblk, nblk),
        in_specs=[qblk, kblk, kblk, pl.BlockSpec((t, LANES), lambda i, j: (i, 0)),
                  pl.BlockSpec((N_HEADS, t), lambda i, j: (0, jnp.minimum(i, j))),
                  pl.BlockSpec((t, D_HALF), lambda i, j: (i, 3))],
        out_specs=[qblk, qblk, pl.BlockSpec((t, LANES), lambda i, j: (i, 0))],
        out_shape=[jax.ShapeDtypeStruct((s, D_HALF), F32), jax.ShapeDtypeStruct((s, D_HALF), F32),
                   jax.ShapeDtypeStruct((s, LANES), F32)],
        scratch_shapes=[pltpu.VMEM((N_HEADS, t, 1), F32), pltpu.VMEM((N_HEADS, t, 1), F32),
                        pltpu.VMEM((t, D_HALF), F32)],
        compiler_params=_params("parallel", "arbitrary"),
    )(q, k, v, cc, cr, u_b)


def _fox_post_bwd(dmix, o, u_b, tm=256):
    s = o.shape[0]

    def body(dm_ref, o_ref, g_ref, do_ref, dg_ref):
        gate = g_ref[...]
        sg = _sigmoid(gate)
        dm = dm_ref[...]
        do_ref[...] = (dm * (gate * sg)).astype(BF16)
        dg_ref[...] = dm * o_ref[...] * (sg * (1.0 + gate * (1.0 - sg)))

    blk = pl.BlockSpec((tm, D_HALF), lambda i: (i, 0))
    return pl.pallas_call(
        body, name="fox_post_bwd", grid=(s // tm,),
        in_specs=[blk, blk, pl.BlockSpec((tm, D_HALF), lambda i: (i, 3))], out_specs=[blk] * 2,
        out_shape=[jax.ShapeDtypeStruct((s, D_HALF), BF16), jax.ShapeDtypeStruct((s, D_HALF), F32)],
        compiler_params=_params("parallel"),
    )(dmix, o, u_b)


def _attn_probs(q2, k2, v2, do2, msk, causal, bias, lse_col):
    zero = jnp.zeros_like(q2)
    qh = jnp.where(msk, q2, zero)
    doh = jnp.where(msk, do2, zero)
    sc = jnp.where(causal, _dot_nt(qh, k2) + bias, NEG)
    pm = jnp.exp(sc - lse_col)
    dp = _dot_nt(doh, v2)
    return qh, doh, pm, dp


def _attn_bwd_rowdot(q, k, v, do, lse, cc, cr):
    s = q.shape[0]
    t = ATT_T
    nblk = s // t

    def body(q_ref, k_ref, v_ref, do_ref, lse_ref, cc_ref, cr_ref, dd_ref, acc):
        i = pl.program_id(0)
        j = pl.program_id(1)

        @pl.when(j == 0)
        def _():
            acc[...] = jnp.zeros_like(acc)

        @pl.when(j <= i)
        def _():
            row = i * t + lax.broadcasted_iota(jnp.int32, (t, t), 0)
            col = j * t + lax.broadcasted_iota(jnp.int32, (t, t), 1)
            causal = row >= col
            left = lax.broadcasted_iota(jnp.int32, (1, LANES), 1) < HEAD
            for p in range(N_PAIRS):
                lanes = slice(p * LANES, (p + 1) * LANES)
                q2, k2, v2, do2 = q_ref[:, lanes], k_ref[:, lanes], v_ref[:, lanes], do_ref[:, lanes]
                for e in range(2):
                    h = 2 * p + e
                    msk = left if e == 0 else jnp.logical_not(left)
                    bias = cc_ref[:, h:h + 1] - cr_ref[h:h + 1, :]
                    _, _, pm, dp = _attn_probs(q2, k2, v2, do2, msk, causal, bias, lse_ref[:, h:h + 1])
                    acc[h] += jnp.sum(pm * dp, axis=1, keepdims=True)

        @pl.when(j == i)
        def _():
            lane = lax.broadcasted_iota(jnp.int32, (1, LANES), 1)
            dd = jnp.zeros((t, LANES), F32)
            for h in range(N_HEADS):
                dd = jnp.where(lane == h, acc[h], dd)
            dd_ref[...] = dd

    qblk = pl.BlockSpec((t, D_HALF), lambda i, j: (i, 0))
    qcol = pl.BlockSpec((t, LANES), lambda i, j: (i, 0))
    kblk = pl.BlockSpec((t, D_HALF), lambda i, j: (jnp.minimum(i, j), 0))
    return pl.pallas_call(
        body, name="fox_attn_rowdot", grid=(nblk, nblk),
        in_specs=[qblk, kblk, kblk, qblk, qcol, qcol, pl.BlockSpec((N_HEADS, t), lambda i, j: (0, jnp.minimum(i, j)))],
        out_specs=qcol, out_shape=jax.ShapeDtypeStruct((s, LANES), F32),
        scratch_shapes=[pltpu.VMEM((N_HEADS, t, 1), F32)],
        compiler_params=_params("parallel", "arbitrary"),
    )(q, k, v, do, lse, cc, cr)


def _attn_bwd(q, k, v, do, lse, dd, cc, cr):
    s = q.shape[0]
    t = ATT_T
    nblk = s // t

    def body(q_ref, k_ref, v_ref, do_ref, lse_ref, dd_ref, cc_ref, cr_ref,
             dq_ref, dk_ref, dv_ref, dcr_ref, dk_sc, dv_sc, dcr_sc):
        j = pl.program_id(0)
        i = pl.program_id(1)

        @pl.when(jnp.logical_and(j == 0, i == 0))
        def _():
            dq_ref[...] = jnp.zeros_like(dq_ref)

        @pl.when(i == 0)
        def _():
            dk_sc[...] = jnp.zeros_like(dk_sc)
            dv_sc[...] = jnp.zeros_like(dv_sc)
            dcr_sc[...] = jnp.zeros_like(dcr_sc)

        @pl.when(i >= j)
        def _():
            row = i * t + lax.broadcasted_iota(jnp.int32, (t, t), 0)
            col = j * t + lax.broadcasted_iota(jnp.int32, (t, t), 1)
            causal = row >= col
            left = lax.broadcasted_iota(jnp.int32, (1, LANES), 1) < HEAD
            qrows = pl.ds(pl.multiple_of(i * t, t), t)
            for p in range(N_PAIRS):
                lanes = slice(p * LANES, (p + 1) * LANES)
                q2, k2, v2, do2 = q_ref[:, lanes], k_ref[:, lanes], v_ref[:, lanes], do_ref[:, lanes]
                zero = jnp.zeros_like(q2)
                dq2 = jnp.zeros((t, LANES), F32)
                dk2 = jnp.zeros((t, LANES), F32)
                dv2 = jnp.zeros((t, LANES), F32)
                for e in range(2):
                    h = 2 * p + e
                    msk = left if e == 0 else jnp.logical_not(left)
                    bias = cc_ref[:, h:h + 1] - cr_ref[h:h + 1, :]
                    qh, doh, pm, dp = _attn_probs(q2, k2, v2, do2, msk, causal, bias, lse_ref[:, h:h + 1])
                    dsc = pm * (dp - dd_ref[:, h:h + 1])
                    dsb = dsc.astype(BF16)
                    dv2 += _dot_tn(pm.astype(BF16), doh)
                    dk2 += _dot_tn(dsb, qh)
                    dq2 += jnp.dot(dsb, jnp.where(msk, k2, zero), preferred_element_type=F32)
                    dcr_sc[h:h + 1, :] += -_colsum(dsc)
                dq_ref[qrows, lanes] += dq2 * ATT_SCALE
                dk_sc[:, lanes] += dk2
                dv_sc[:, lanes] += dv2

        @pl.when(i == nblk - 1)
        def _():
            dk_ref[...] = dk_sc[...]
            dv_ref[...] = dv_sc[...]
            dcr_ref[...] = dcr_sc[...]

    qblk = pl.BlockSpec((t, D_HALF), lambda j, i: (jnp.maximum(i, j), 0))
    qcol = pl.BlockSpec((t, LANES), lambda j, i: (jnp.maximum(i, j), 0))
    kblk = pl.BlockSpec((t, D_HALF), lambda j, i: (j, 0))
    return pl.pallas_call(
        body, name="fox_attn_bwd", grid=(nblk, nblk),
        in_specs=[qblk, kblk, kblk, qblk, qcol, qcol, qcol, pl.BlockSpec((N_HEADS, t), lambda j, i: (0, j))],
        out_specs=[pl.BlockSpec((s, D_HALF), lambda j, i: (0, 0)), kblk, kblk,
                   pl.BlockSpec((N_HEADS, t), lambda j, i: (0, j))],
        out_shape=[jax.ShapeDtypeStruct((s, D_HALF), F32)] * 3 + [jax.ShapeDtypeStruct((N_HEADS, s), F32)],
        scratch_shapes=[pltpu.VMEM((t, D_HALF), F32), pltpu.VMEM((t, D_HALF), F32), pltpu.VMEM((N_HEADS, t), F32)],
        compiler_params=_params("arbitrary", "arbitrary"),
    )(q, k, v, do, lse, dd, cc, cr)


def _fox_prep_bwd(u_b, dq, dk, dv, dgate, dcum, fb, qg, kg, tm=256):
    s = u_b.shape[0]
    nb = s // tm

    def body(ub_ref, dq_ref, dk_ref, dv_ref, dg_ref, dc_ref, fb_ref, qg_ref, kg_ref,
             du_ref, dqg_ref, dkg_ref, dfb_ref, carry):
        i = pl.program_id(0)

        @pl.when(i == 0)
        def _():
            carry[...] = jnp.zeros_like(carry)
            dqg_ref[...] = jnp.zeros_like(dqg_ref)
            dkg_ref[...] = jnp.zeros_like(dkg_ref)
            dfb_ref[...] = jnp.zeros_like(dfb_ref)

        bd = _head_ones()
        for lo, g_ref, d_ref, dgain_ref in ((0, qg_ref, dq_ref, dqg_ref), (512, kg_ref, dk_ref, dkg_ref)):
            gain = g_ref[...]
            xh, rinv, _ = _head_rms(ub_ref[:, lo:lo + 512], gain, bd)
            dn = d_ref[...]
            dgain_ref[...] += _colsum(dn * xh)
            dxh = dn * gain
            du_ref[:, lo:lo + 512] = rinv * (dxh - xh * (_head_sum(dxh * xh, bd) * (1.0 / HEAD)))
        du_ref[:, 1024:1536] = dv_ref[...]
        du_ref[:, 1536:2048] = dg_ref[...]
        lane = lax.broadcasted_iota(jnp.int32, (1, LANES), 1)
        dc = dc_ref[...]
        dlogf = _exact_dot(dc, _tri(tm, False), ones_first=True) + carry[...]
        carry[...] += _colsum(dc)
        fl = ub_ref[:, 2048:2176] + fb_ref[...]
        dfl = jnp.where(lane < N_HEADS, dlogf * (1.0 - _sigmoid(fl)), 0.0)
        du_ref[:, 2048:2176] = dfl
        dfb_ref[...] += _colsum(dfl)

    rev = lambda w: pl.BlockSpec((tm, w), lambda i: (nb - 1 - i, 0))
    vec = lambda w: pl.BlockSpec((1, w), lambda i: (0, 0))
    return pl.pallas_call(
        body, name="fox_prep_bwd", grid=(nb,),
        in_specs=[rev(SEC)] + [rev(D_HALF)] * 4 + [rev(LANES), vec(LANES), vec(D_HALF), vec(D_HALF)],
        out_specs=[rev(SEC), vec(D_HALF), vec(D_HALF), vec(LANES)],
        out_shape=[jax.ShapeDtypeStruct((s, SEC), F32), jax.ShapeDtypeStruct((1, D_HALF), F32),
                   jax.ShapeDtypeStruct((1, D_HALF), F32), jax.ShapeDtypeStruct((1, LANES), F32)],
        scratch_shapes=[pltpu.VMEM((1, LANES), F32)],
        compiler_params=_params("arbitrary"),
    )(u_b, dq, dk, dv, dgate, dcum, fb, qg, kg)


def _merge(mix_a, mix_b, u_g, x, tgt, wa, wb, wo, fg, tm=256):
    s, d = x.shape

    def body(ma_ref, mb_ref, ug_ref, x_ref, t_ref, wa_ref, wb_ref, wo_ref, fg_ref,
             dx2_ref, dma_ref, dmb_ref, dug_ref, dwa_ref, dwb_ref, dwo_ref, dfg_ref, loss_ref):
        i = pl.program_id(0)

        @pl.when(i == 0)
        def _():
            for ref in (dwa_ref, dwb_ref, dwo_ref, dfg_ref, loss_ref):
                ref[...] = jnp.zeros_like(ref)

        wa_v, wb_v, wo_v, fg_v = wa_ref[...], wb_ref[...], wo_ref[...], fg_ref[...]
        ma = ma_ref[...].astype(BF16)
        mb = mb_ref[...].astype(BF16)
        ya = jnp.dot(ma, wa_v, preferred_element_type=F32)
        yb = jnp.dot(mb, wb_v, preferred_element_type=F32)
        sa = _sigmoid(ug_ref[:, 0:d])
        sb = _sigmoid(ug_ref[:, d:2 * d])
        merged = (sa * ya + sb * yb).astype(BF16)
        x2 = x_ref[...] + jnp.dot(merged, wo_v, preferred_element_type=F32)
        r2 = lax.rsqrt(jnp.mean(x2 * x2, axis=-1, keepdims=True) + RMS_EPS)
        x2h = x2 * r2
        err = x2h * fg_v - t_ref[...]
        loss_ref[...] += _colsum(err * err)
        dy = err * (1.0 / d)
        dfg_ref[...] += _colsum(dy * x2h)
        dx2h = dy * fg_v
        dx2 = r2 * (dx2h - x2h * jnp.mean(dx2h * x2h, axis=-1, keepdims=True))
        dx2_ref[...] = dx2
        dx2b = dx2.astype(BF16)
        dmerged = _dot_nt(dx2b, wo_v)
        dwo_ref[...] += _dot_tn(merged, dx2b)
        dya = dmerged * sa
        dyb = dmerged * sb
        dug_ref[:, 0:d] = dya * ya * (1.0 - sa)
        dug_ref[:, d:2 * d] = dyb * yb * (1.0 - sb)
        dyab = dya.astype(BF16)
        dybb = dyb.astype(BF16)
        dma_ref[...] = _dot_nt(dyab, wa_v)
        dmb_ref[...] = _dot_nt(dybb, wb_v)
        dwa_ref[...] += _dot_tn(ma, dyab)
        dwb_ref[...] += _dot_tn(mb, dybb)

    row = lambda w: pl.BlockSpec((tm, w), lambda i: (i, 0))
    full = lambda a: pl.BlockSpec(a.shape, lambda i: (0, 0))
    fshape = lambda a: jax.ShapeDtypeStruct(a.shape, F32)
    return pl.pallas_call(
        body, name="merge_fwd_bwd", grid=(s // tm,),
        in_specs=[row(D_HALF), row(D_HALF), row(GATE_COLS), row(d), row(d), full(wa), full(wb), full(wo), full(fg)],
        out_specs=[row(d), row(D_HALF), row(D_HALF), row(GATE_COLS), full(wa), full(wb), full(wo), full(fg), full(fg)],
        out_shape=[jax.ShapeDtypeStruct((s, d), F32), jax.ShapeDtypeStruct((s, D_HALF), F32),
                   jax.ShapeDtypeStruct((s, D_HALF), F32), jax.ShapeDtypeStruct((s, GATE_COLS), F32),
                   fshape(wa), fshape(wb), fshape(wo), fshape(fg), fshape(fg)],
        compiler_params=_params("arbitrary"),
    )(mix_a, mix_b, u_g, x, tgt, wa, wb, wo, fg)


def _lora_weight(w_up, a_up):
    z = jnp.zeros((LORA, D_HALF), w_up.dtype)
    return jnp.concatenate([jnp.concatenate([w_up, z], axis=1), jnp.concatenate([z, a_up], axis=1)], axis=0)


def _device_grads(x, tgt, norm_g, w_a, w_b, w_g, shift_mu, w_up, w0, a_up, a0, k_k, k_a, r_k, lnx_w, lnx_b,
                  f_bias, q_norm_g, k_norm_g, w_out_a, w_out_b, w_out, final_norm_g):
    wl = _lora_weight(w_up, a_up)
    rk = r_k.reshape(1, D_HALF)
    fb = jnp.pad(f_bias, ((0, 0), (0, LANES - N_HEADS)))
    qg = jnp.tile(q_norm_g, (1, N_HEADS))
    kg = jnp.tile(k_norm_g, (1, N_HEADS))
    fg = final_norm_g.reshape(1, D_MODEL)

    h = _rmsnorm_in(x, norm_g)
    u_a = _matmul_nn(h, w_a, "inproj_rwkv")
    u_b = _matmul_nn(h, w_b, "inproj_fox")
    u_g = _matmul_nn(h, w_g, "inproj_gate")

    r, dec, k, v, av, bv, gate_a = _rwkv_prep(u_a, shift_mu, wl, w0, a0, k_k, k_a)
    v_t = v.T
    y, st = _wkv_fwd(r, dec, k, av, bv, v_t)
    mix_a = _rwkv_post(y, r, k, v, gate_a, lnx_w, lnx_b, rk)

    q, kn, vb, cc, cr = _fox_prep(u_b, fb, qg, kg)
    o, mix_b, lse = _attn_fwd(q, kn, vb, cc, cr, u_b)

    dx2, dmix_a, dmix_b, du_g, dwa, dwb, dwo, dfg, loss_vec = _merge(
        mix_a, mix_b, u_g, x, tgt, w_out_a, w_out_b, w_out, fg)

    do, dgate_b = _fox_post_bwd(dmix_b, o, u_b)
    dd = _attn_bwd_rowdot(q, kn, vb, do, lse, cc, cr)
    dq, dk_att, dv_att, dcr = _attn_bwd(q, kn, vb, do, lse, dd, cc, cr)
    dcum = jnp.pad(dcr.T, ((0, 0), (0, LANES - N_HEADS)))
    du_b, dqg, dkg, dfb = _fox_prep_bwd(u_b, dq, dk_att, dv_att, dgate_b, dcum, fb, qg, kg)

    dy, dr_b, dk_b, dv_b, dgate_a, dlw, dlb, drk = _rwkv_post_bwd(dmix_a, y, r, k, v, gate_a, lnx_w, lnx_b, rk)
    dr_s, dw_s, dk_s, dv_s, da_s, db_s = _wkv_bwd(r, dec, k, av, bv, v_t, dy.T, st)
    du_a, dmu, dwl, dw0, da0, dkkw, dkaw = _rwkv_prep_bwd(
        u_a, (dr_s, dw_s, dk_s, dv_s, da_s, db_s, dr_b, dk_b, dv_b, dgate_a), shift_mu, wl, w0, a0, k_k, k_a)

    h_t = h.T
    dw_a = _matmul_tn_acc(h_t, du_a, "dw_rwkv")
    dw_b = _matmul_tn_acc(h_t, du_b, "dw_fox")
    dw_g = _matmul_tn_acc(h_t, du_g, "dw_gate")
    grad_x, dnorm_g = _inproj_bwd(du_a, du_b, du_g, w_a, w_b, w_g, x, dx2, norm_g)

    grads = dict(
        norm_g=dnorm_g, w_in=jnp.concatenate([dw_a, dw_b[:, :FOX_REAL], dw_g], axis=1), shift_mu=dmu,
        w_lora_up=dwl[:LORA, :D_HALF], w0=dw0, a_lora_up=dwl[LORA:, D_HALF:], a0=da0, k_k=dkkw, k_a=dkaw,
        r_k=drk.reshape(1, N_HEADS, HEAD), lnx_w=dlw, lnx_b=dlb, f_bias=dfb[:, :N_HEADS],
        q_norm_g=dqg.reshape(N_HEADS, HEAD).sum(axis=0, keepdims=True),
        k_norm_g=dkg.reshape(N_HEADS, HEAD).sum(axis=0, keepdims=True),
        w_out_a=dwa, w_out_b=dwb, w_out=dwo, final_norm_g=dfg.reshape(D_MODEL))
    return loss_vec, grad_x, grads


CHIP_FLIPS = ((1, 0), (0, 1), (1, 1))
ANY = pl.BlockSpec(memory_space=pl.ANY)


def _position():
    return lax.axis_index("x"), lax.axis_index("y"), lax.axis_index("c")


def _flip(v, f):
    return 1 - v if f else v


def _gather_shards(shards):
    n = len(shards)
    n_remote = len(CHIP_FLIPS) * n

    def body(*refs):
        ins, outs = refs[:n], refs[n:2 * n]
        send_sems, recv_sems, local_sems = refs[2 * n:]
        x, y, c = _position()
        me = 2 * x + y
        local = [pltpu.make_async_copy(ins[t], outs[t].at[me], local_sems.at[t]) for t in range(n)]
        for cp in local:
            cp.start()
        sends, recvs = [], []
        for f, (fx, fy) in enumerate(CHIP_FLIPS):
            px, py = _flip(x, fx), _flip(y, fy)
            for t in range(n):
                sems = dict(send_sem=send_sems.at[f * n + t], recv_sem=recv_sems.at[f * n + t],
                            device_id=(px, py, c), device_id_type=MESH)
                sends.append(pltpu.make_async_remote_copy(src_ref=ins[t], dst_ref=outs[t].at[me], **sems))
                recvs.append(pltpu.make_async_remote_copy(src_ref=ins[t], dst_ref=outs[t].at[2 * px + py], **sems))
        for cp in sends:
            cp.start()
        for cp in recvs:
            cp.wait_recv()
        for cp in sends:
            cp.wait_send()
        for cp in local:
            cp.wait()

    return pl.pallas_call(
        body, name="gather_weights", in_specs=[ANY] * n, out_specs=[ANY] * n,
        out_shape=[jax.ShapeDtypeStruct((N_CHIPS,) + a.shape, a.dtype) for a in shards],
        scratch_shapes=[pltpu.SemaphoreType.DMA((n_remote,)), pltpu.SemaphoreType.DMA((n_remote,)),
                        pltpu.SemaphoreType.DMA((n,))],
        compiler_params=pltpu.CompilerParams(has_side_effects=True),
    )(*shards)


def _scatter_partials(stacks):
    n = len(stacks)
    n_remote = len(CHIP_FLIPS) * n

    def body(*refs):
        ins, outs = refs[:n], refs[n:2 * n]
        send_sems, recv_sems = refs[2 * n:]
        x, y, c = _position()
        sends, recvs = [], []
        for f, (fx, fy) in enumerate(CHIP_FLIPS):
            px, py = _flip(x, fx), _flip(y, fy)
            for t in range(n):
                cp = pltpu.make_async_remote_copy(
                    src_ref=ins[t].at[2 * px + py], dst_ref=outs[t].at[f],
                    send_sem=send_sems.at[f * n + t], recv_sem=recv_sems.at[f * n + t],
                    device_id=(px, py, c), device_id_type=MESH)
                sends.append(cp)
        for cp in sends:
            cp.start()
        for cp in sends:
            cp.wait_recv()
        for cp in sends:
            cp.wait_send()

    return pl.pallas_call(
        body, name="scatter_partials", in_specs=[ANY] * n, out_specs=[ANY] * n,
        out_shape=[jax.ShapeDtypeStruct((len(CHIP_FLIPS),) + a.shape[1:], a.dtype) for a in stacks],
        scratch_shapes=[pltpu.SemaphoreType.DMA((n_remote,)), pltpu.SemaphoreType.DMA((n_remote,))],
        compiler_params=pltpu.CompilerParams(has_side_effects=True),
    )(*stacks)


def _swap_sibling(tensors):
    n = len(tensors)

    def body(*refs):
        ins, outs = refs[:n], refs[n:2 * n]
        send_sems, recv_sems = refs[2 * n:]
        x, y, c = _position()
        copies = [pltpu.make_async_remote_copy(
            src_ref=ins[t], dst_ref=outs[t], send_sem=send_sems.at[t], recv_sem=recv_sems.at[t],
            device_id=(x, y, 1 - c), device_id_type=MESH) for t in range(n)]
        for cp in copies:
            cp.start()
        for cp in copies:
            cp.wait_recv()
        for cp in copies:
            cp.wait_send()

    return pl.pallas_call(
        body, name="swap_sibling", in_specs=[ANY] * n, out_specs=[ANY] * n,
        out_shape=[jax.ShapeDtypeStruct(a.shape, a.dtype) for a in tensors],
        scratch_shapes=[pltpu.SemaphoreType.DMA((n,)), pltpu.SemaphoreType.DMA((n,))],
        compiler_params=pltpu.CompilerParams(has_side_effects=True),
    )(*tensors)


def _allreduce_small(slab):
    stages = 3

    def body(x_ref, o_ref, buf, send_sems, recv_sems):
        x, y, c = _position()
        peers = ((1 - x, y, c), (x, 1 - y, c), (x, y, 1 - c))
        o_ref[...] = x_ref[...]
        for k, peer in enumerate(peers):
            cp = pltpu.make_async_remote_copy(src_ref=o_ref, dst_ref=buf.at[k], send_sem=send_sems.at[k],
                                              recv_sem=recv_sems.at[k], device_id=peer, device_id_type=MESH)
            cp.start()
            cp.wait()
            o_ref[...] = o_ref[...] + buf[k]

    return pl.pallas_call(
        body, name="allreduce_small",
        in_specs=[pl.BlockSpec(memory_space=pltpu.VMEM)], out_specs=pl.BlockSpec(memory_space=pltpu.VMEM),
        out_shape=jax.ShapeDtypeStruct(slab.shape, slab.dtype),
        scratch_shapes=[pltpu.VMEM((stages,) + slab.shape, slab.dtype),
                        pltpu.SemaphoreType.DMA((stages,)), pltpu.SemaphoreType.DMA((stages,))],
        compiler_params=pltpu.CompilerParams(has_side_effects=True),
    )(slab)


def _row_tile(r):
    return min(r, 256)


def _sum4(stack, recv, me):
    _, r, c = stack.shape
    tr = _row_tile(r)

    def body(me_ref, own_ref, recv_ref, o_ref):
        o_ref[...] = ((own_ref[...] + recv_ref[0]) + recv_ref[1]) + recv_ref[2]

    return pl.pallas_call(
        body, name="sum_partials",
        grid_spec=pltpu.PrefetchScalarGridSpec(
            num_scalar_prefetch=1, grid=(r // tr,),
            in_specs=[pl.BlockSpec((None, tr, c), lambda i, me_ref: (me_ref[0], i, 0)),
                      pl.BlockSpec((len(CHIP_FLIPS), tr, c), lambda i, me_ref: (0, i, 0))],
            out_specs=pl.BlockSpec((tr, c), lambda i, me_ref: (i, 0))),
        out_shape=jax.ShapeDtypeStruct((r, c), F32), compiler_params=_params("parallel"),
    )(me, stack, recv)


def _adamw_math(w, g, m, v):
    m = ADAM_B1 * m + (1.0 - ADAM_B1) * g
    v = ADAM_B2 * v + (1.0 - ADAM_B2) * (g * g)
    m_hat = m / (1.0 - ADAM_B1 ** ADAM_STEP)
    v_hat = v / (1.0 - ADAM_B2 ** ADAM_STEP)
    delta = -ADAM_LR * (m_hat / (jnp.sqrt(v_hat) + ADAM_EPS) + ADAM_WD * w)
    return delta, m, v


def _adamw(w, m, v, g_parts, name):
    r, c = w.shape
    tr = _row_tile(r)
    n = len(g_parts)

    def body(*refs):
        w_ref, m_ref, v_ref = refs[:3]
        g_refs = refs[3:3 + n]
        g_out, d_out, m_out, v_out = refs[3 + n:]
        g = g_refs[0][...]
        for ref in g_refs[1:]:
            g = g + ref[...]
        g_out[...] = g
        d_out[...], m_out[...], v_out[...] = _adamw_math(w_ref[...], g, m_ref[...], v_ref[...])

    blk = pl.BlockSpec((tr, c), lambda i: (i, 0))
    return pl.pallas_call(
        body, name=name, grid=(r // tr,), in_specs=[blk] * (3 + n), out_specs=[blk] * 4,
        out_shape=[jax.ShapeDtypeStruct((r, c), F32)] * 4, compiler_params=_params("parallel"),
    )(w, m, v, *g_parts)


SHARDED = ("w_in", "w_lora_up", "a_lora_up", "w_out_a", "w_out_b", "w_out")
ROW_SHARDED = ("w_out",)
SMALL = ("norm_g", "shift_mu", "w0", "a0", "k_k", "k_a", "r_k", "lnx_w", "lnx_b", "f_bias", "q_norm_g", "k_norm_g",
         "final_norm_g")
WEIGHTS = ("norm_g", "w_in", "shift_mu", "w_lora_up", "w0", "a_lora_up", "a0", "k_k", "k_a", "r_k", "lnx_w", "lnx_b",
           "f_bias", "q_norm_g", "k_norm_g", "w_out_a", "w_out_b", "w_out", "final_norm_g")
SLAB_ROWS = 16
SLAB_COLS = SEC


def _to_slab(named, extra=None):
    rows = [jnp.pad(named[n].reshape(1, -1), ((0, 0), (0, SLAB_COLS - named[n].size))) for n in SMALL]
    if extra is not None:
        rows.append(jnp.pad(extra.reshape(1, -1), ((0, 0), (0, SLAB_COLS - extra.size))))
    rows.append(jnp.zeros((SLAB_ROWS - len(rows), SLAB_COLS), F32))
    return jnp.concatenate(rows, axis=0)


def _from_slab(slab, shapes):
    return {n: slab[i, :math.prod(shapes[n])].reshape(shapes[n]) for i, n in enumerate(SMALL)}


def _by_chip(g, name):
    if name in ROW_SHARDED:
        return g.reshape(N_CHIPS, g.shape[0] // N_CHIPS, g.shape[1])
    r, c = g.shape
    return g.reshape(r, N_CHIPS, c // N_CHIPS).transpose(1, 0, 2)


def _from_chips(stack, name):
    if name in ROW_SHARDED:
        return stack.reshape(-1, stack.shape[2])
    _, r, c = stack.shape
    return stack.transpose(1, 0, 2).reshape(r, N_CHIPS * c)


def kernel(x, norm_g, w_in, shift_mu, w_lora_up, w0, a_lora_up, a0, k_k, k_a, r_k, lnx_w, lnx_b, f_bias, q_norm_g, k_norm_g, w_out_a, w_out_b, w_out, final_norm_g, loss_target, m_norm_g, m_w_in, m_shift_mu, m_w_lora_up, m_w0, m_a_lora_up, m_a0, m_k_k, m_k_a, m_r_k, m_lnx_w, m_lnx_b, m_f_bias, m_q_norm_g, m_k_norm_g, m_w_out_a, m_w_out_b, m_w_out, m_final_norm_g, v_norm_g, v_w_in, v_shift_mu, v_w_lora_up, v_w0, v_a_lora_up, v_a0, v_k_k, v_k_a, v_r_k, v_lnx_w, v_lnx_b, v_f_bias, v_q_norm_g, v_k_norm_g, v_w_out_a, v_w_out_b, v_w_out, v_final_norm_g):
    w = dict(norm_g=norm_g, w_in=w_in, shift_mu=shift_mu, w_lora_up=w_lora_up, w0=w0, a_lora_up=a_lora_up, a0=a0,
             k_k=k_k, k_a=k_a, r_k=r_k, lnx_w=lnx_w, lnx_b=lnx_b, f_bias=f_bias, q_norm_g=q_norm_g,
             k_norm_g=k_norm_g, w_out_a=w_out_a, w_out_b=w_out_b, w_out=w_out, final_norm_g=final_norm_g)
    m = dict(norm_g=m_norm_g, w_in=m_w_in, shift_mu=m_shift_mu, w_lora_up=m_w_lora_up, w0=m_w0,
             a_lora_up=m_a_lora_up, a0=m_a0, k_k=m_k_k, k_a=m_k_a, r_k=m_r_k, lnx_w=m_lnx_w, lnx_b=m_lnx_b,
             f_bias=m_f_bias, q_norm_g=m_q_norm_g, k_norm_g=m_k_norm_g, w_out_a=m_w_out_a, w_out_b=m_w_out_b,
             w_out=m_w_out, final_norm_g=m_final_norm_g)
    v = dict(norm_g=v_norm_g, w_in=v_w_in, shift_mu=v_shift_mu, w_lora_up=v_w_lora_up, w0=v_w0,
             a_lora_up=v_a_lora_up, a0=v_a0, k_k=v_k_k, k_a=v_k_a, r_k=v_r_k, lnx_w=v_lnx_w, lnx_b=v_lnx_b,
             f_bias=v_f_bias, q_norm_g=v_q_norm_g, k_norm_g=v_k_norm_g, w_out_a=v_w_out_a, w_out_b=v_w_out_b,
             w_out=v_w_out, final_norm_g=v_final_norm_g)
    shapes = {n: w[n].shape for n in WEIGHTS}

    gathered = _gather_shards([w[n][0].astype(BF16) for n in SHARDED])
    full = {n: _from_chips(g, n) for n, g in zip(SHARDED, gathered)}
    w_full = full["w_in"]
    w_a = w_full[:, :RWKV_COLS]
    w_b = jnp.pad(w_full[:, RWKV_COLS:RWKV_COLS + FOX_REAL], ((0, 0), (0, SEC - FOX_REAL)))
    w_g = w_full[:, RWKV_COLS + FOX_REAL:]

    loss_vec, grad_x, grads = _device_grads(
        x[0], loss_target[0], norm_g, w_a, w_b, w_g, shift_mu, full["w_lora_up"], w0, full["a_lora_up"], a0, k_k, k_a,
        r_k, lnx_w, lnx_b, f_bias, q_norm_g, k_norm_g, full["w_out_a"], full["w_out_b"], full["w_out"], final_norm_g)

    total = _allreduce_small(_to_slab(grads, extra=loss_vec))
    loss = (0.5 / D_MODEL) * jnp.sum(total[len(SMALL)])
    slab_g, slab_d, slab_m, slab_v = _adamw(_to_slab(w), _to_slab(m), _to_slab(v), [total], "adamw_small")
    out_g, out_d, out_m, out_v = (_from_slab(s, shapes) for s in (total, slab_d, slab_m, slab_v))
    del slab_g

    xpos, ypos, _ = _position()
    me = (2 * xpos + ypos).astype(jnp.int32).reshape(1)
    stacks = [_by_chip(grads[n], n) for n in SHARDED]
    received = _scatter_partials(stacks)
    core_sums = [_sum4(s, r, me) for s, r in zip(stacks, received)]
    sibling_sums = _swap_sibling(core_sums)
    for n, mine, theirs in zip(SHARDED, core_sums, sibling_sums):
        g, d, m2, v2 = _adamw(w[n][0], m[n][0], v[n][0], [mine, theirs], "adamw_" + n)
        out_g[n], out_d[n], out_m[n], out_v[n] = (a.reshape(shapes[n]) for a in (g, d, m2, v2))

    return (loss, grad_x.reshape(x.shape), *[out_g[n] for n in WEIGHTS], *[out_d[n] for n in WEIGHTS],
            *[out_m[n] for n in WEIGHTS], *[out_v[n] for n in WEIGHTS])
```

```python
import functools
import math

import jax
import jax.numpy as jnp
from jax import lax
from jax.experimental import pallas as pl
from jax.experimental.pallas import tpu as pltpu

F32 = jnp.float32
BF16 = jnp.bfloat16

D_MODEL = 1024
D_HALF = 512
HEAD = 64
N_HEADS = 8
LORA = 64
RWKV_COLS = 2176
FOX_REAL = 2056
SEC = 2176
GATE_COLS = 2048
IN_COLS = 6280
N_CHIPS = 4
SHARD_COLS = IN_COLS // N_CHIPS
RMS_EPS = 1e-6
LNX_EPS = 64e-5
ATT_SCALE = HEAD ** -0.5
NEG = -1e30

ADAM_LR = 0.001
ADAM_B1 = 0.9
ADAM_B2 = 0.999
ADAM_EPS = 1e-08
ADAM_WD = 0.01
ADAM_STEP = 10

LANES = 128
SUBLANES = 8
VMEM_LIMIT = 56 * 1024 * 1024
MESH = pl.DeviceIdType.MESH


def _params(*sem):
    return pltpu.CompilerParams(dimension_semantics=sem if sem else None, vmem_limit_bytes=VMEM_LIMIT)


def _sigmoid(x):
    return 1.0 / (1.0 + jnp.exp(-x))


def _log_sigmoid(x):
    return jnp.minimum(x, 0.0) - jnp.log(1.0 + jnp.exp(-jnp.abs(x)))


def _head_ones():
    r = lax.broadcasted_iota(jnp.int32, (LANES, LANES), 0) >> 6
    c = lax.broadcasted_iota(jnp.int32, (LANES, LANES), 1) >> 6
    return (r == c).astype(BF16)


def _split3(x):
    hi = x.astype(BF16)
    r1 = x - hi.astype(F32)
    mid = r1.astype(BF16)
    lo = (r1 - mid.astype(F32)).astype(BF16)
    return hi, mid, lo


def _exact_dot(x, ones_bf16, ones_first=False):
    out = None
    for piece in _split3(x):
        if ones_first:
            t = jnp.dot(ones_bf16, piece, preferred_element_type=F32)
        else:
            t = jnp.dot(piece, ones_bf16, preferred_element_type=F32)
        out = t if out is None else out + t
    return out


def _head_sum(x, bd):
    n = x.shape[1] // LANES
    parts = [_exact_dot(x[:, i * LANES:(i + 1) * LANES], bd) for i in range(n)]
    return parts[0] if n == 1 else jnp.concatenate(parts, axis=1)


def _dot_nt(a, b):
    return lax.dot_general(a, b, (((1,), (1,)), ((), ())), preferred_element_type=F32)


def _dot_tn(a, b):
    return lax.dot_general(a, b, (((0,), (0,)), ((), ())), preferred_element_type=F32)


def _colsum(x):
    return jnp.sum(x, axis=0, keepdims=True)


def _rmsnorm_in(x, g, tm=512):
    s, d = x.shape

    def body(x_ref, g_ref, h_ref):
        xv = x_ref[...]
        r = lax.rsqrt(jnp.mean(xv * xv, axis=-1, keepdims=True) + RMS_EPS)
        h_ref[...] = (xv * r * g_ref[...]).astype(BF16)

    return pl.pallas_call(
        body, name="rmsnorm_in", grid=(s // tm,),
        in_specs=[pl.BlockSpec((tm, d), lambda i: (i, 0)), pl.BlockSpec((1, d), lambda i: (0, 0))],
        out_specs=pl.BlockSpec((tm, d), lambda i: (i, 0)),
        out_shape=jax.ShapeDtypeStruct((s, d), BF16), compiler_params=_params("parallel"),
    )(x, g)


def _matmul_nn(a, b, name, tm=512):
    m, k = a.shape
    n = b.shape[1]

    def body(a_ref, b_ref, o_ref):
        o_ref[...] = jnp.dot(a_ref[...], b_ref[...], preferred_element_type=F32)

    return pl.pallas_call(
        body, name=name, grid=(m // tm,),
        in_specs=[pl.BlockSpec((tm, k), lambda i: (i, 0)), pl.BlockSpec((k, n), lambda i: (0, 0))],
        out_specs=pl.BlockSpec((tm, n), lambda i: (i, 0)),
        out_shape=jax.ShapeDtypeStruct((m, n), F32), compiler_params=_params("parallel"),
    )(a, b)


def _matmul_tn_acc(at, b, name, tk=512):
    m, k = at.shape
    n = b.shape[1]

    def body(a_ref, b_ref, o_ref):
        j = pl.program_id(0)

        @pl.when(j == 0)
        def _():
            o_ref[...] = jnp.zeros_like(o_ref)

        o_ref[...] += jnp.dot(a_ref[...], b_ref[...].astype(BF16), preferred_element_type=F32)

    return pl.pallas_call(
        body, name=name, grid=(k // tk,),
        in_specs=[pl.BlockSpec((m, tk), lambda j: (0, j)), pl.BlockSpec((tk, n), lambda j: (j, 0))],
        out_specs=pl.BlockSpec((m, n), lambda j: (0, 0)),
        out_shape=jax.ShapeDtypeStruct((m, n), F32), compiler_params=_params("arbitrary"),
    )(at, b)


def _inproj_bwd(du_a, du_b, du_g, w_a, w_b, w_g, x, dx2, g, tm=256):
    s, d = x.shape

    def body(da_ref, db_ref, dg_ref, wa_ref, wb_ref, wg_ref, x_ref, dx2_ref, g_ref, gx_ref, gg_ref):
        i = pl.program_id(0)

        @pl.when(i == 0)
        def _():
            gg_ref[...] = jnp.zeros_like(gg_ref)

        dh = _dot_nt(da_ref[...].astype(BF16), wa_ref[...])
        dh += _dot_nt(db_ref[...].astype(BF16), wb_ref[...])
        dh += _dot_nt(dg_ref[...].astype(BF16), wg_ref[...])
        xv = x_ref[...]
        r = lax.rsqrt(jnp.mean(xv * xv, axis=-1, keepdims=True) + RMS_EPS)
        xh = xv * r
        gg_ref[...] += _colsum(dh * xh)
        dxh = dh * g_ref[...]
        gx_ref[...] = dx2_ref[...] + r * (dxh - xh * jnp.mean(dxh * xh, axis=-1, keepdims=True))

    row = lambda w: pl.BlockSpec((tm, w), lambda i: (i, 0))
    full = lambda a: pl.BlockSpec(a.shape, lambda i: (0, 0))
    return pl.pallas_call(
        body, name="inproj_bwd", grid=(s // tm,),
        in_specs=[row(SEC), row(SEC), row(GATE_COLS), full(w_a), full(w_b), full(w_g), row(d), row(d), full(g)],
        out_specs=[row(d), pl.BlockSpec((1, d), lambda i: (0, 0))],
        out_shape=[jax.ShapeDtypeStruct((s, d), F32), jax.ShapeDtypeStruct((1, d), F32)],
        compiler_params=_params("arbitrary"),
    )(du_a, du_b, du_g, w_a, w_b, w_g, x, dx2, g)


def _rwkv_elementwise(ua, prev_row, first, mu, wl, w0, a0, kkw, kaw, bd):
    tm = ua.shape[0]
    rows = lax.broadcasted_iota(jnp.int32, (tm, 1), 0)
    prev = jnp.where(first, jnp.zeros_like(prev_row), prev_row)
    shifted = jnp.where(rows == 0, prev, pltpu.roll(ua, 1, 0))
    delta = shifted - ua
    us = ua + delta * mu
    r = us[:, 0:512]
    k0 = us[:, 512:1024]
    v = us[:, 1024:1536]
    lo = us[:, 1536:1664]
    gate = us[:, 1664:2176]
    lane = lax.broadcasted_iota(jnp.int32, (1, LANES), 1)
    th = jnp.tanh(lo)
    lin = jnp.where(lane < LORA, th, lo)
    ll = jnp.dot(lin.astype(BF16), wl, preferred_element_type=F32)
    sz = _sigmoid(w0 + ll[:, :512])
    e = sz * math.exp(-0.5)
    dec = jnp.exp(-e)
    a = _sigmoid(a0 + ll[:, 512:])
    kk0 = k0 * kkw
    ss = _head_sum(kk0 * kk0, bd)
    nrm = jnp.maximum(jnp.sqrt(ss), 1e-12)
    kk = kk0 / nrm
    k = k0 * (1.0 + (a - 1.0) * kaw)
    return dict(delta=delta, us=us, r=r, k0=k0, v=v, lo=lo, gate=gate, th=th, lin=lin, sz=sz, e=e, dec=dec,
                a=a, kk0=kk0, ss=ss, nrm=nrm, kk=kk, k=k)


def _rwkv_prep(u_a, mu, wl, w0, a0, kkw, kaw, tm=256):
    s = u_a.shape[0]

    def body(ua_ref, prev_ref, mu_ref, wl_ref, w0_ref, a0_ref, kkw_ref, kaw_ref,
             r_ref, w_ref, k_ref, v_ref, a_ref, b_ref, g_ref):
        i = pl.program_id(0)
        f = _rwkv_elementwise(ua_ref[...], prev_ref[7:8, :], i == 0, mu_ref[...], wl_ref[...], w0_ref[...],
                              a0_ref[...], kkw_ref[...], kaw_ref[...], _head_ones())
        r_ref[...] = f["r"]
        w_ref[...] = f["dec"]
        k_ref[...] = f["k"]
        v_ref[...] = f["v"]
        a_ref[...] = -f["kk"]
        b_ref[...] = f["kk"] * f["a"]
        g_ref[...] = f["gate"]

    vec = lambda w: pl.BlockSpec((1, w), lambda i: (0, 0))
    out = pl.BlockSpec((tm, D_HALF), lambda i: (i, 0))
    return pl.pallas_call(
        body, name="rwkv_prep", grid=(s // tm,),
        in_specs=[pl.BlockSpec((tm, SEC), lambda i: (i, 0)),
                  pl.BlockSpec((8, SEC), lambda i: (jnp.maximum(i * (tm // 8) - 1, 0), 0)),
                  vec(SEC), pl.BlockSpec((LANES, 2 * D_HALF), lambda i: (0, 0)),
                  vec(D_HALF), vec(D_HALF), vec(D_HALF), vec(D_HALF)],
        out_specs=[out] * 7,
        out_shape=[jax.ShapeDtypeStruct((s, D_HALF), F32)] * 7,
        compiler_params=_params("parallel"),
    )(u_a, u_a, mu, wl, w0, a0, kkw, kaw)


SCAN_TB = 64
N_PAIRS = 4


def _pair_sum(x, left):
    s_l = jnp.sum(jnp.where(left, x, 0.0), axis=1, keepdims=True)
    s_r = jnp.sum(jnp.where(left, 0.0, x), axis=1, keepdims=True)
    return jnp.where(left, s_l, s_r)


def _quad_consts():
    lane = lax.broadcasted_iota(jnp.int32, (HEAD, 2 * LANES), 1)
    rowi = lax.broadcasted_iota(jnp.int32, (HEAD, 2 * LANES), 0)
    diag2 = rowi == (lane & (HEAD - 1))
    r = lax.broadcasted_iota(jnp.int32, (2 * LANES, 2 * LANES), 0) >> 6
    c = lax.broadcasted_iota(jnp.int32, (2 * LANES, 2 * LANES), 1) >> 6
    return diag2, (r == c).astype(BF16)


def _rows_to_columns(x8, diag2, bd2):
    hi = x8.astype(BF16).astype(F32)
    lo = x8 - hi
    lhs = jnp.concatenate([jnp.where(diag2, piece[i:i + 1], 0.0).astype(BF16)
                           for piece in (hi, lo) for i in range(SUBLANES)], axis=0)
    res = jnp.dot(lhs, bd2, preferred_element_type=F32)
    half = SUBLANES * HEAD
    return [res[i * HEAD:(i + 1) * HEAD] + res[half + i * HEAD:half + (i + 1) * HEAD] for i in range(SUBLANES)]


def _left_half():
    return lax.broadcasted_iota(jnp.int32, (HEAD, LANES), 1) < HEAD


def _wkv_fwd(r, w, k, a, b, v):
    s = r.shape[0]
    tb = SCAN_TB

    def body(r_ref, w_ref, k_ref, a_ref, b_ref, v_ref, y_ref, st_ref, state, qbuf):
        g = pl.program_id(0)

        @pl.when(g == 0)
        def _():
            state[...] = jnp.zeros_like(state)

        left = _left_half()
        diag2, bd2 = _quad_consts()
        sub_row = lax.broadcasted_iota(jnp.int32, (SUBLANES, 2 * LANES), 0)
        half = SUBLANES * HEAD

        def group(q, carry):
            rows8 = pl.ds(pl.multiple_of(q * SUBLANES, SUBLANES), SUBLANES)
            a8, w8, b8, k8, r8, v8 = (x[rows8, :] for x in (a_ref, w_ref, b_ref, k_ref, r_ref, v_ref))
            vb = [_rows_to_columns(v8[:, g2 * 2 * LANES:(g2 + 1) * 2 * LANES], diag2, bd2) for g2 in range(2)]
            sp = [state[p] for p in range(N_PAIRS)]
            for i in range(SUBLANES):
                row = slice(i, i + 1)
                for p in range(N_PAIRS):
                    st_ref[q * SUBLANES + i, p] = sp[p]
                sa = [_pair_sum(sp[p] * a8[row, p * LANES:(p + 1) * LANES], left) for p in range(N_PAIRS)]
                for p in range(N_PAIRS):
                    lanes = slice(p * LANES, (p + 1) * LANES)
                    inner = slice((p % 2) * LANES, (p % 2 + 1) * LANES)
                    sp[p] = sp[p] * w8[row, lanes] + sa[p] * b8[row, lanes] + vb[p // 2][i][:, inner] * k8[row, lanes]
                    qv = sp[p] * r8[row, lanes]
                    hi = qv.astype(BF16)
                    qbuf[p // 2, i * HEAD:(i + 1) * HEAD, inner] = hi
                    qbuf[p // 2, half + i * HEAD:half + (i + 1) * HEAD, inner] = (qv - hi.astype(F32)).astype(BF16)
            for p in range(N_PAIRS):
                state[p] = sp[p]
            for g2 in range(2):
                res = jnp.dot(qbuf[g2], bd2, preferred_element_type=F32)
                y8 = jnp.zeros((SUBLANES, 2 * LANES), F32)
                for i in range(SUBLANES):
                    yb = res[i * HEAD:(i + 1) * HEAD] + res[half + i * HEAD:half + (i + 1) * HEAD]
                    y8 = jnp.where(sub_row == i, _colsum(jnp.where(diag2, yb, 0.0)), y8)
                y_ref[rows8, g2 * 2 * LANES:(g2 + 1) * 2 * LANES] = y8
            return carry

        lax.fori_loop(0, tb // SUBLANES, group, 0)

    rows = pl.BlockSpec((tb, D_HALF), lambda g: (g, 0))
    return pl.pallas_call(
        body, name="wkv_fwd", grid=(s // tb,),
        in_specs=[rows] * 6,
        out_specs=[rows, pl.BlockSpec((tb, N_PAIRS, HEAD, LANES), lambda g: (g, 0, 0, 0))],
        out_shape=[jax.ShapeDtypeStruct((s, D_HALF), F32),
                   jax.ShapeDtypeStruct((s, N_PAIRS, HEAD, LANES), F32)],
        scratch_shapes=[pltpu.VMEM((N_PAIRS, HEAD, LANES), F32),
                        pltpu.VMEM((2, 2 * SUBLANES * HEAD, 2 * LANES), BF16)],
        compiler_params=_params("arbitrary"),
    )(r, w, k, a, b, v)


def _wkv_bwd(r, w, k, a, b, v, dy, st):
    s = r.shape[0]
    tb = SCAN_TB
    nb = s // tb

    def body(r_ref, w_ref, k_ref, a_ref, b_ref, v_ref, dy_ref, st_ref,
             dr_ref, dw_ref, dk_ref, dv_ref, da_ref, db_ref, dstate, qbuf):
        g = pl.program_id(0)

        @pl.when(g == 0)
        def _():
            dstate[...] = jnp.zeros_like(dstate)

        left = _left_half()
        diag2, bd2 = _quad_consts()
        sub_row = lax.broadcasted_iota(jnp.int32, (SUBLANES, LANES), 0)
        sub_row2 = lax.broadcasted_iota(jnp.int32, (SUBLANES, 2 * LANES), 0)
        half = SUBLANES * HEAD
        row_refs = (dr_ref, dw_ref, dk_ref, da_ref, db_ref)

        def group(qq, carry):
            q = tb // SUBLANES - 1 - qq
            rows8 = pl.ds(pl.multiple_of(q * SUBLANES, SUBLANES), SUBLANES)
            a8, w8, b8, k8, r8, v8, dy8 = (x[rows8, :] for x in (a_ref, w_ref, b_ref, k_ref, r_ref, v_ref, dy_ref))
            quads = [slice(g2 * 2 * LANES, (g2 + 1) * 2 * LANES) for g2 in range(2)]
            vb = [_rows_to_columns(v8[:, qd], diag2, bd2) for qd in quads]
            dyb = [_rows_to_columns(dy8[:, qd], diag2, bd2) for qd in quads]
            dsp = [dstate[p] for p in range(N_PAIRS)]
            outs = [[jnp.zeros((SUBLANES, LANES), F32) for _ in row_refs] for _ in range(N_PAIRS)]
            for i in reversed(range(SUBLANES)):
                row = slice(i, i + 1)
                pl_ = [slice(p * LANES, (p + 1) * LANES) for p in range(N_PAIRS)]
                inner = [slice((p % 2) * LANES, (p % 2 + 1) * LANES) for p in range(N_PAIRS)]
                sp = [st_ref[q * SUBLANES + i, p] for p in range(N_PAIRS)]
                dyt = [dyb[p // 2][i][:, inner[p]] for p in range(N_PAIRS)]
                ds = [dsp[p] + dyt[p] * r8[row, pl_[p]] for p in range(N_PAIRS)]
                dsa = [_pair_sum(ds[p] * b8[row, pl_[p]], left) for p in range(N_PAIRS)]
                sa = [_pair_sum(sp[p] * a8[row, pl_[p]], left) for p in range(N_PAIRS)]
                for p in range(N_PAIRS):
                    ar, wr, br, kr = (x[row, pl_[p]] for x in (a8, w8, b8, k8))
                    vt = vb[p // 2][i][:, inner[p]]
                    dsp[p] = ds[p] * wr + dsa[p] * ar
                    sn = sp[p] * wr + sa[p] * br + vt * kr
                    new = (_colsum(sn * dyt[p]), _colsum(ds[p] * sp[p]), _colsum(ds[p] * vt),
                           _colsum(sp[p] * dsa[p]), _colsum(ds[p] * sa[p]))
                    outs[p] = [jnp.where(sub_row == i, n, o) for n, o in zip(new, outs[p])]
                    xv = ds[p] * kr
                    hi = xv.astype(BF16)
                    qbuf[p // 2, i * HEAD:(i + 1) * HEAD, inner[p]] = hi
                    qbuf[p // 2, half + i * HEAD:half + (i + 1) * HEAD, inner[p]] = (xv - hi.astype(F32)).astype(BF16)
            for p in range(N_PAIRS):
                dstate[p] = dsp[p]
                for ref, o in zip(row_refs, outs[p]):
                    ref[rows8, p * LANES:(p + 1) * LANES] = o
            for g2 in range(2):
                res = jnp.dot(qbuf[g2], bd2, preferred_element_type=F32)
                dv8 = jnp.zeros((SUBLANES, 2 * LANES), F32)
                for i in range(SUBLANES):
                    t = res[i * HEAD:(i + 1) * HEAD] + res[half + i * HEAD:half + (i + 1) * HEAD]
                    dv8 = jnp.where(sub_row2 == i, _colsum(jnp.where(diag2, t, 0.0)), dv8)
                dv_ref[rows8, quads[g2]] = dv8
            return carry

        lax.fori_loop(0, tb // SUBLANES, group, 0)

    rows = pl.BlockSpec((tb, D_HALF), lambda g: (nb - 1 - g, 0))
    return pl.pallas_call(
        body, name="wkv_bwd", grid=(nb,),
        in_specs=[rows] * 7 + [pl.BlockSpec((tb, N_PAIRS, HEAD, LANES), lambda g: (nb - 1 - g, 0, 0, 0))],
        out_specs=[rows] * 6,
        out_shape=[jax.ShapeDtypeStruct((s, D_HALF), F32)] * 6,
        scratch_shapes=[pltpu.VMEM((N_PAIRS, HEAD, LANES), F32),
                        pltpu.VMEM((2, 2 * SUBLANES * HEAD, 2 * LANES), BF16)],
        compiler_params=_params("arbitrary"),
    )(r, w, k, a, b, v, dy, st)


def _rwkv_post_math(y, r, k, v, gate, lw, lb, rk, bd):
    mean = _head_sum(y, bd) * (1.0 / HEAD)
    yc = y - mean
    var = _head_sum(yc * yc, bd) * (1.0 / HEAD)
    rstd = lax.rsqrt(var + LNX_EPS)
    yn = yc * rstd
    rkk = _head_sum(r * k * rk, bd)
    sg = _sigmoid(gate)
    pre = yn * lw + lb + rkk * v
    return yn, rstd, rkk, sg, pre


def _rwkv_post(y, r, k, v, gate, lw, lb, rk, tm=256):
    s = y.shape[0]

    def body(y_ref, r_ref, k_ref, v_ref, g_ref, lw_ref, lb_ref, rk_ref, o_ref):
        gate_v = g_ref[...]
        _, _, _, sg, pre = _rwkv_post_math(y_ref[...], r_ref[...], k_ref[...], v_ref[...], gate_v,
                                           lw_ref[...], lb_ref[...], rk_ref[...], _head_ones())
        o_ref[...] = pre * (gate_v * sg)

    blk = pl.BlockSpec((tm, D_HALF), lambda i: (i, 0))
    vec = pl.BlockSpec((1, D_HALF), lambda i: (0, 0))
    return pl.pallas_call(
        body, name="rwkv_post", grid=(s // tm,),
        in_specs=[blk] * 5 + [vec] * 3, out_specs=blk,
        out_shape=jax.ShapeDtypeStruct((s, D_HALF), F32), compiler_params=_params("parallel"),
    )(y, r, k, v, gate, lw, lb, rk)


def _rwkv_post_bwd(dmix, y, r, k, v, gate, lw, lb, rk, tm=256):
    s = y.shape[0]

    def body(dm_ref, y_ref, r_ref, k_ref, v_ref, g_ref, lw_ref, lb_ref, rk_ref,
             dy_ref, dr_ref, dk_ref, dv_ref, dg_ref, dlw_ref, dlb_ref, drk_ref):
        i = pl.program_id(0)

        @pl.when(i == 0)
        def _():
            dlw_ref[...] = jnp.zeros_like(dlw_ref)
            dlb_ref[...] = jnp.zeros_like(dlb_ref)
            drk_ref[...] = jnp.zeros_like(drk_ref)

        bd = _head_ones()
        rv, kv, vv, gate_v, lw_v, rk_v = r_ref[...], k_ref[...], v_ref[...], g_ref[...], lw_ref[...], rk_ref[...]
        yn, rstd, rkk, sg, pre = _rwkv_post_math(y_ref[...], rv, kv, vv, gate_v, lw_v, lb_ref[...], rk_v, bd)
        dm = dm_ref[...]
        dg_ref[...] = dm * pre * (sg * (1.0 + gate_v * (1.0 - sg)))
        dpre = dm * (gate_v * sg)
        dlw_ref[...] += _colsum(dpre * yn)
        dlb_ref[...] += _colsum(dpre)
        dyn = dpre * lw_v
        m1 = _head_sum(dyn, bd) * (1.0 / HEAD)
        m2 = _head_sum(dyn * yn, bd) * (1.0 / HEAD)
        dy_ref[...] = rstd * (dyn - m1 - yn * m2)
        dv_ref[...] = dpre * rkk
        drkk = _head_sum(dpre * vv, bd)
        dr_ref[...] = drkk * kv * rk_v
        dk_ref[...] = drkk * rv * rk_v
        drk_ref[...] += _colsum(drkk * rv * kv)

    blk = pl.BlockSpec((tm, D_HALF), lambda i: (i, 0))
    vec = pl.BlockSpec((1, D_HALF), lambda i: (0, 0))
    return pl.pallas_call(
        body, name="rwkv_post_bwd", grid=(s // tm,),
        in_specs=[blk] * 6 + [vec] * 3, out_specs=[blk] * 5 + [vec] * 3,
        out_shape=[jax.ShapeDtypeStruct((s, D_HALF), F32)] * 5 + [jax.ShapeDtypeStruct((1, D_HALF), F32)] * 3,
        compiler_params=_params("arbitrary"),
    )(dmix, y, r, k, v, gate, lw, lb, rk)


def _rwkv_prep_bwd(u_a, grads, mu, wl, w0, a0, kkw, kaw, tm=256):
    s = u_a.shape[0]
    nb = s // tm

    def body(ua_ref, prev_ref, drs_ref, dws_ref, dks_ref, dvs_ref, das_ref, dbs_ref, drb_ref, dkb_ref, dvb_ref,
             dgt_ref, mu_ref, wl_ref, w0_ref, a0_ref, kkw_ref, kaw_ref,
             du_ref, dmu_ref, dwl_ref, dw0_ref, da0_ref, dkkw_ref, dkaw_ref, carry):
        i = pl.program_id(0)

        @pl.when(i == 0)
        def _():
            carry[...] = jnp.zeros_like(carry)
            for ref in (dmu_ref, dwl_ref, dw0_ref, da0_ref, dkkw_ref, dkaw_ref):
                ref[...] = jnp.zeros_like(ref)

        bd = _head_ones()
        mu_v, wl_v, kkw_v, kaw_v = mu_ref[...], wl_ref[...], kkw_ref[...], kaw_ref[...]
        f = _rwkv_elementwise(ua_ref[...], prev_ref[7:8, :], i == nb - 1, mu_v, wl_v, w0_ref[...],
                              a0_ref[...], kkw_v, kaw_v, bd)
        a, kk, k0 = f["a"], f["kk"], f["k0"]
        dk = dks_ref[...] + dkb_ref[...]
        dbs = dbs_ref[...]
        dkk = dbs * a - das_ref[...]
        da = dbs * kk + dk * k0 * kaw_v
        dk0 = dk * (1.0 + (a - 1.0) * kaw_v)
        dkaw_ref[...] += _colsum(dk * k0 * (a - 1.0))
        inv = 1.0 / f["nrm"]
        proj = _head_sum(dkk * kk, bd)
        dkk0 = jnp.where(f["ss"] > 1e-24, (dkk - kk * proj) * inv, dkk * inv)
        dk0 = dk0 + dkk0 * kkw_v
        dkkw_ref[...] += _colsum(dkk0 * k0)
        dza = da * a * (1.0 - a)
        da0_ref[...] += _colsum(dza)
        dz = -dws_ref[...] * f["dec"] * f["e"] * (1.0 - f["sz"])
        dw0_ref[...] += _colsum(dz)
        dll = jnp.concatenate([dz, dza], axis=1).astype(BF16)
        dwl_ref[...] += _dot_tn(f["lin"].astype(BF16), dll)
        dlin = _dot_nt(dll, wl_v)
        lane = lax.broadcasted_iota(jnp.int32, (1, LANES), 1)
        th = f["th"]
        dlo = jnp.where(lane < LORA, dlin * (1.0 - th * th), dlin)
        dus = jnp.concatenate([drs_ref[...] + drb_ref[...], dk0, dvs_ref[...] + dvb_ref[...], dlo, dgt_ref[...]],
                              axis=1)
        dmu_ref[...] += _colsum(dus * f["delta"])
        g1 = dus * mu_v
        rows = lax.broadcasted_iota(jnp.int32, (tm, 1), 0)
        up = jnp.where(rows == tm - 1, carry[...], pltpu.roll(g1, tm - 1, 0))
        du_ref[...] = dus - g1 + up
        carry[...] = g1[0:1, :]

    rev = lambda w: pl.BlockSpec((tm, w), lambda i: (nb - 1 - i, 0))
    vec = lambda w: pl.BlockSpec((1, w), lambda i: (0, 0))
    wl_spec = pl.BlockSpec((LANES, 2 * D_HALF), lambda i: (0, 0))
    return pl.pallas_call(
        body, name="rwkv_prep_bwd", grid=(nb,),
        in_specs=[rev(SEC), pl.BlockSpec((8, SEC), lambda i: (jnp.maximum((nb - 1 - i) * (tm // 8) - 1, 0), 0))]
                 + [rev(D_HALF)] * 10 + [vec(SEC), wl_spec] + [vec(D_HALF)] * 4,
        out_specs=[rev(SEC), vec(SEC), wl_spec] + [vec(D_HALF)] * 4,
        out_shape=[jax.ShapeDtypeStruct((s, SEC), F32), jax.ShapeDtypeStruct((1, SEC), F32),
                   jax.ShapeDtypeStruct((LANES, 2 * D_HALF), F32)] + [jax.ShapeDtypeStruct((1, D_HALF), F32)] * 4,
        scratch_shapes=[pltpu.VMEM((1, SEC), F32)],
        compiler_params=_params("arbitrary"),
    )(u_a, u_a, *grads, mu, wl, w0, a0, kkw, kaw)


def _tri(tm, lower):
    r = lax.broadcasted_iota(jnp.int32, (tm, tm), 0)
    c = lax.broadcasted_iota(jnp.int32, (tm, tm), 1)
    return ((r >= c) if lower else (r <= c)).astype(BF16)


def _head_rms(x, g, bd):
    rinv = lax.rsqrt(_head_sum(x * x, bd) * (1.0 / HEAD) + RMS_EPS)
    xh = x * rinv
    return xh, rinv, xh * g


def _fox_prep(u_b, fb, qg, kg, tm=256):
    s = u_b.shape[0]

    def body(ub_ref, fb_ref, qg_ref, kg_ref, q_ref, k_ref, v_ref, cc_ref, cr_ref, carry):
        i = pl.program_id(0)

        @pl.when(i == 0)
        def _():
            carry[...] = jnp.zeros_like(carry)

        bd = _head_ones()
        _, _, qn = _head_rms(ub_ref[:, 0:512], qg_ref[...], bd)
        _, _, kn = _head_rms(ub_ref[:, 512:1024], kg_ref[...], bd)
        q_ref[...] = (qn * ATT_SCALE).astype(BF16)
        k_ref[...] = kn.astype(BF16)
        v_ref[...] = ub_ref[:, 1024:1536].astype(BF16)
        lane = lax.broadcasted_iota(jnp.int32, (1, LANES), 1)
        logf = jnp.where(lane < N_HEADS, _log_sigmoid(ub_ref[:, 2048:2176] + fb_ref[...]), 0.0)
        cum = _exact_dot(logf, _tri(tm, True), ones_first=True) + carry[...]
        cc_ref[...] = cum
        cr_ref[...] = jnp.transpose(cum)[0:N_HEADS, :]
        carry[...] = cum[tm - 1:tm, :]

    blk = pl.BlockSpec((tm, D_HALF), lambda i: (i, 0))
    return pl.pallas_call(
        body, name="fox_prep", grid=(s // tm,),
        in_specs=[pl.BlockSpec((tm, SEC), lambda i: (i, 0)), pl.BlockSpec((1, LANES), lambda i: (0, 0)),
                  pl.BlockSpec((1, D_HALF), lambda i: (0, 0)), pl.BlockSpec((1, D_HALF), lambda i: (0, 0))],
        out_specs=[blk, blk, blk, pl.BlockSpec((tm, LANES), lambda i: (i, 0)),
                   pl.BlockSpec((N_HEADS, tm), lambda i: (0, i))],
        out_shape=[jax.ShapeDtypeStruct((s, D_HALF), BF16)] * 3
                  + [jax.ShapeDtypeStruct((s, LANES), F32), jax.ShapeDtypeStruct((N_HEADS, s), F32)],
        scratch_shapes=[pltpu.VMEM((1, LANES), F32)],
        compiler_params=_params("arbitrary"),
    )(u_b, fb, qg, kg)


ATT_T = 256


def _attn_fwd(q, k, v, cc, cr, u_b):
    s = q.shape[0]
    t = ATT_T
    nblk = s // t

    def body(q_ref, k_ref, v_ref, cc_ref, cr_ref, g_ref, o_ref, mix_ref, lse_ref, m_sc, l_sc, acc_sc):
        i = pl.program_id(0)
        j = pl.program_id(1)

        @pl.when(j == 0)
        def _():
            m_sc[...] = jnp.full_like(m_sc, NEG)
            l_sc[...] = jnp.zeros_like(l_sc)
            acc_sc[...] = jnp.zeros_like(acc_sc)

        @pl.when(j <= i)
        def _():
            row = i * t + lax.broadcasted_iota(jnp.int32, (t, t), 0)
            col = j * t + lax.broadcasted_iota(jnp.int32, (t, t), 1)
            causal = row >= col
            left = lax.broadcasted_iota(jnp.int32, (1, LANES), 1) < HEAD
            for p in range(N_PAIRS):
                lanes = slice(p * LANES, (p + 1) * LANES)
                q2, k2, v2 = q_ref[:, lanes], k_ref[:, lanes], v_ref[:, lanes]
                acc2 = acc_sc[:, lanes]
                for e in range(2):
                    h = 2 * p + e
                    msk = left if e == 0 else jnp.logical_not(left)
                    sc = _dot_nt(jnp.where(msk, q2, jnp.zeros_like(q2)), k2)
                    sc = sc + (cc_ref[:, h:h + 1] - cr_ref[h:h + 1, :])
                    sc = jnp.where(causal, sc, NEG)
                    m_prev = m_sc[h]
                    m_new = jnp.maximum(m_prev, jnp.max(sc, axis=1, keepdims=True))
                    alpha = jnp.exp(m_prev - m_new)
                    pm = jnp.exp(sc - m_new)
                    l_sc[h] = alpha * l_sc[h] + jnp.sum(pm, axis=1, keepdims=True)
                    m_sc[h] = m_new
                    pv = jnp.dot(pm.astype(BF16), v2, preferred_element_type=F32)
                    acc2 = jnp.where(msk, alpha * acc2 + pv, acc2)
                acc_sc[:, lanes] = acc2

        @pl.when(j == i)
        def _():
            lane = lax.broadcasted_iota(jnp.int32, (1, LANES), 1)
            left = lane < HEAD
            lse = jnp.zeros((t, LANES), F32)
            for p in range(N_PAIRS):
                lanes = slice(p * LANES, (p + 1) * LANES)
                inv = jnp.where(left, 1.0 / l_sc[2 * p], 1.0 / l_sc[2 * p + 1])
                o = acc_sc[:, lanes] * inv
                o_ref[:, lanes] = o
                gate = g_ref[:, lanes]
                mix_ref[:, lanes] = o * (gate * _sigmoid(gate))
                for e in range(2):
                    h = 2 * p + e
                    lse = jnp.where(lane == h, m_sc[h] + jnp.log(l_sc[h]), lse)
            lse_ref[...] = lse

    qblk = pl.BlockSpec((t, D_HALF), lambda i, j: (i, 0))
    kblk = pl.BlockSpec((t, D_HALF), lambda i, j: (jnp.minimum(i, j), 0))
    return pl.pallas_call(
        body, name="fox_attn_fwd", grid=(nblk, nblk),
        in_specs=[qblk, kblk, kblk, pl.BlockSpec((t, LANES), lambda i, j: (i, 0)),
                  pl.BlockSpec((N_HEADS, t), lambda i, j: (0, jnp.minimum(i, j))),
                  pl.BlockSpec((t, D_HALF), lambda i, j: (i, 3))],
        out_specs=[qblk, qblk, pl.BlockSpec((t, LANES), lambda i, j: (i, 0))],
        out_shape=[jax.ShapeDtypeStruct((s, D_HALF), F32), jax.ShapeDtypeStruct((s, D_HALF), F32),
                   jax.ShapeDtypeStruct((s, LANES), F32)],
        scratch_shapes=[pltpu.VMEM((N_HEADS, t, 1), F32), pltpu.VMEM((N_HEADS, t, 1), F32),
                        pltpu.VMEM((t, D_HALF), F32)],
        compiler_params=_params("parallel", "arbitrary"),
    )(q, k, v, cc, cr, u_b)


def _fox_post_bwd(dmix, o, u_b, tm=256):
    s = o.shape[0]

    def body(dm_ref, o_ref, g_ref, do_ref, dg_ref):
        gate = g_ref[...]
        sg = _sigmoid(gate)
        dm = dm_ref[...]
        do_ref[...] = (dm * (gate * sg)).astype(BF16)
        dg_ref[...] = dm * o_ref[...] * (sg * (1.0 + gate * (1.0 - sg)))

    blk = pl.BlockSpec((tm, D_HALF), lambda i: (i, 0))
    return pl.pallas_call(
        body, name="fox_post_bwd", grid=(s // tm,),
        in_specs=[blk, blk, pl.BlockSpec((tm, D_HALF), lambda i: (i, 3))], out_specs=[blk] * 2,
        out_shape=[jax.ShapeDtypeStruct((s, D_HALF), BF16), jax.ShapeDtypeStruct((s, D_HALF), F32)],
        compiler_params=_params("parallel"),
    )(dmix, o, u_b)


def _attn_probs(q2, k2, v2, do2, msk, causal, bias, lse_col):
    zero = jnp.zeros_like(q2)
    qh = jnp.where(msk, q2, zero)
    doh = jnp.where(msk, do2, zero)
    sc = jnp.where(causal, _dot_nt(qh, k2) + bias, NEG)
    pm = jnp.exp(sc - lse_col)
    dp = _dot_nt(doh, v2)
    return qh, doh, pm, dp


def _attn_bwd_rowdot(q, k, v, do, lse, cc, cr):
    s = q.shape[0]
    t = ATT_T
    nblk = s // t

    def body(q_ref, k_ref, v_ref, do_ref, lse_ref, cc_ref, cr_ref, dd_ref, acc):
        i = pl.program_id(0)
        j = pl.program_id(1)

        @pl.when(j == 0)
        def _():
            acc[...] = jnp.zeros_like(acc)

        @pl.when(j <= i)
        def _():
            row = i * t + lax.broadcasted_iota(jnp.int32, (t, t), 0)
            col = j * t + lax.broadcasted_iota(jnp.int32, (t, t), 1)
            causal = row >= col
            left = lax.broadcasted_iota(jnp.int32, (1, LANES), 1) < HEAD
            for p in range(N_PAIRS):
                lanes = slice(p * LANES, (p + 1) * LANES)
                q2, k2, v2, do2 = q_ref[:, lanes], k_ref[:, lanes], v_ref[:, lanes], do_ref[:, lanes]
                for e in range(2):
                    h = 2 * p + e
                    msk = left if e == 0 else jnp.logical_not(left)
                    bias = cc_ref[:, h:h + 1] - cr_ref[h:h + 1, :]
                    _, _, pm, dp = _attn_probs(q2, k2, v2, do2, msk, causal, bias, lse_ref[:, h:h + 1])
                    acc[h] += jnp.sum(pm * dp, axis=1, keepdims=True)

        @pl.when(j == i)
        def _():
            lane = lax.broadcasted_iota(jnp.int32, (1, LANES), 1)
            dd = jnp.zeros((t, LANES), F32)
            for h in range(N_HEADS):
                dd = jnp.where(lane == h, acc[h], dd)
            dd_ref[...] = dd

    qblk = pl.BlockSpec((t, D_HALF), lambda i, j: (i, 0))
    qcol = pl.BlockSpec((t, LANES), lambda i, j: (i, 0))
    kblk = pl.BlockSpec((t, D_HALF), lambda i, j: (jnp.minimum(i, j), 0))
    return pl.pallas_call(
        body, name="fox_attn_rowdot", grid=(nblk, nblk),
        in_specs=[qblk, kblk, kblk, qblk, qcol, qcol, pl.BlockSpec((N_HEADS, t), lambda i, j: (0, jnp.minimum(i, j)))],
        out_specs=qcol, out_shape=jax.ShapeDtypeStruct((s, LANES), F32),
        scratch_shapes=[pltpu.VMEM((N_HEADS, t, 1), F32)],
        compiler_params=_params("parallel", "arbitrary"),
    )(q, k, v, do, lse, cc, cr)


def _attn_bwd(q, k, v, do, lse, dd, cc, cr):
    s = q.shape[0]
    t = ATT_T
    nblk = s // t

    def body(q_ref, k_ref, v_ref, do_ref, lse_ref, dd_ref, cc_ref, cr_ref,
             dq_ref, dk_ref, dv_ref, dcr_ref, dk_sc, dv_sc, dcr_sc):
        j = pl.program_id(0)
        i = pl.program_id(1)

        @pl.when(jnp.logical_and(j == 0, i == 0))
        def _():
            dq_ref[...] = jnp.zeros_like(dq_ref)

        @pl.when(i == 0)
        def _():
            dk_sc[...] = jnp.zeros_like(dk_sc)
            dv_sc[...] = jnp.zeros_like(dv_sc)
            dcr_sc[...] = jnp.zeros_like(dcr_sc)

        @pl.when(i >= j)
        def _():
            row = i * t + lax.broadcasted_iota(jnp.int32, (t, t), 0)
            col = j * t + lax.broadcasted_iota(jnp.int32, (t, t), 1)
            causal = row >= col
            left = lax.broadcasted_iota(jnp.int32, (1, LANES), 1) < HEAD
            qrows = pl.ds(pl.multiple_of(i * t, t), t)
            for p in range(N_PAIRS):
                lanes = slice(p * LANES, (p + 1) * LANES)
                q2, k2, v2, do2 = q_ref[:, lanes], k_ref[:, lanes], v_ref[:, lanes], do_ref[:, lanes]
                zero = jnp.zeros_like(q2)
                dq2 = jnp.zeros((t, LANES), F32)
                dk2 = jnp.zeros((t, LANES), F32)
                dv2 = jnp.zeros((t, LANES), F32)
                for e in range(2):
                    h = 2 * p + e
                    msk = left if e == 0 else jnp.logical_not(left)
                    bias = cc_ref[:, h:h + 1] - cr_ref[h:h + 1, :]
                    qh, doh, pm, dp = _attn_probs(q2, k2, v2, do2, msk, causal, bias, lse_ref[:, h:h + 1])
                    dsc = pm * (dp - dd_ref[:, h:h + 1])
                    dsb = dsc.astype(BF16)
                    dv2 += _dot_tn(pm.astype(BF16), doh)
                    dk2 += _dot_tn(dsb, qh)
                    dq2 += jnp.dot(dsb, jnp.where(msk, k2, zero), preferred_element_type=F32)
                    dcr_sc[h:h + 1, :] += -_colsum(dsc)
                dq_ref[qrows, lanes] += dq2 * ATT_SCALE
                dk_sc[:, lanes] += dk2
                dv_sc[:, lanes] += dv2

        @pl.when(i == nblk - 1)
        def _():
            dk_ref[...] = dk_sc[...]
            dv_ref[...] = dv_sc[...]
            dcr_ref[...] = dcr_sc[...]

    qblk = pl.BlockSpec((t, D_HALF), lambda j, i: (jnp.maximum(i, j), 0))
    qcol = pl.BlockSpec((t, LANES), lambda j, i: (jnp.maximum(i, j), 0))
    kblk = pl.BlockSpec((t, D_HALF), lambda j, i: (j, 0))
    return pl.pallas_call(
        body, name="fox_attn_bwd", grid=(nblk, nblk),
        in_specs=[qblk, kblk, kblk, qblk, qcol, qcol, qcol, pl.BlockSpec((N_HEADS, t), lambda j, i: (0, j))],
        out_specs=[pl.BlockSpec((s, D_HALF), lambda j, i: (0, 0)), kblk, kblk,
                   pl.BlockSpec((N_HEADS, t), lambda j, i: (0, j))],
        out_shape=[jax.ShapeDtypeStruct((s, D_HALF), F32)] * 3 + [jax.ShapeDtypeStruct((N_HEADS, s), F32)],
        scratch_shapes=[pltpu.VMEM((t, D_HALF), F32), pltpu.VMEM((t, D_HALF), F32), pltpu.VMEM((N_HEADS, t), F32)],
        compiler_params=_params("arbitrary", "arbitrary"),
    )(q, k, v, do, lse, dd, cc, cr)


def _fox_prep_bwd(u_b, dq, dk, dv, dgate, dcum, fb, qg, kg, tm=256):
    s = u_b.shape[0]
    nb = s // tm

    def body(ub_ref, dq_ref, dk_ref, dv_ref, dg_ref, dc_ref, fb_ref, qg_ref, kg_ref,
             du_ref, dqg_ref, dkg_ref, dfb_ref, carry):
        i = pl.program_id(0)

        @pl.when(i == 0)
        def _():
            carry[...] = jnp.zeros_like(carry)
            dqg_ref[...] = jnp.zeros_like(dqg_ref)
            dkg_ref[...] = jnp.zeros_like(dkg_ref)
            dfb_ref[...] = jnp.zeros_like(dfb_ref)

        bd = _head_ones()
        for lo, g_ref, d_ref, dgain_ref in ((0, qg_ref, dq_ref, dqg_ref), (512, kg_ref, dk_ref, dkg_ref)):
            gain = g_ref[...]
            xh, rinv, _ = _head_rms(ub_ref[:, lo:lo + 512], gain, bd)
            dn = d_ref[...]
            dgain_ref[...] += _colsum(dn * xh)
            dxh = dn * gain
            du_ref[:, lo:lo + 512] = rinv * (dxh - xh * (_head_sum(dxh * xh, bd) * (1.0 / HEAD)))
        du_ref[:, 1024:1536] = dv_ref[...]
        du_ref[:, 1536:2048] = dg_ref[...]
        lane = lax.broadcasted_iota(jnp.int32, (1, LANES), 1)
        dc = dc_ref[...]
        dlogf = _exact_dot(dc, _tri(tm, False), ones_first=True) + carry[...]
        carry[...] += _colsum(dc)
        fl = ub_ref[:, 2048:2176] + fb_ref[...]
        dfl = jnp.where(lane < N_HEADS, dlogf * (1.0 - _sigmoid(fl)), 0.0)
        du_ref[:, 2048:2176] = dfl
        dfb_ref[...] += _colsum(dfl)

    rev = lambda w: pl.BlockSpec((tm, w), lambda i: (nb - 1 - i, 0))
    vec = lambda w: pl.BlockSpec((1, w), lambda i: (0, 0))
    return pl.pallas_call(
        body, name="fox_prep_bwd", grid=(nb,),
        in_specs=[rev(SEC)] + [rev(D_HALF)] * 4 + [rev(LANES), vec(LANES), vec(D_HALF), vec(D_HALF)],
        out_specs=[rev(SEC), vec(D_HALF), vec(D_HALF), vec(LANES)],
        out_shape=[jax.ShapeDtypeStruct((s, SEC), F32), jax.ShapeDtypeStruct((1, D_HALF), F32),
                   jax.ShapeDtypeStruct((1, D_HALF), F32), jax.ShapeDtypeStruct((1, LANES), F32)],
        scratch_shapes=[pltpu.VMEM((1, LANES), F32)],
        compiler_params=_params("arbitrary"),
    )(u_b, dq, dk, dv, dgate, dcum, fb, qg, kg)


def _merge(mix_a, mix_b, u_g, x, tgt, wa, wb, wo, fg, tm=256):
    s, d = x.shape

    def body(ma_ref, mb_ref, ug_ref, x_ref, t_ref, wa_ref, wb_ref, wo_ref, fg_ref,
             dx2_ref, dma_ref, dmb_ref, dug_ref, dwa_ref, dwb_ref, dwo_ref, dfg_ref, loss_ref):
        i = pl.program_id(0)

        @pl.when(i == 0)
        def _():
            for ref in (dwa_ref, dwb_ref, dwo_ref, dfg_ref, loss_ref):
                ref[...] = jnp.zeros_like(ref)

        wa_v, wb_v, wo_v, fg_v = wa_ref[...], wb_ref[...], wo_ref[...], fg_ref[...]
        ma = ma_ref[...].astype(BF16)
        mb = mb_ref[...].astype(BF16)
        ya = jnp.dot(ma, wa_v, preferred_element_type=F32)
        yb = jnp.dot(mb, wb_v, preferred_element_type=F32)
        sa = _sigmoid(ug_ref[:, 0:d])
        sb = _sigmoid(ug_ref[:, d:2 * d])
        merged = (sa * ya + sb * yb).astype(BF16)
        x2 = x_ref[...] + jnp.dot(merged, wo_v, preferred_element_type=F32)
        r2 = lax.rsqrt(jnp.mean(x2 * x2, axis=-1, keepdims=True) + RMS_EPS)
        x2h = x2 * r2
        err = x2h * fg_v - t_ref[...]
        loss_ref[...] += _colsum(err * err)
        dy = err * (1.0 / d)
        dfg_ref[...] += _colsum(dy * x2h)
        dx2h = dy * fg_v
        dx2 = r2 * (dx2h - x2h * jnp.mean(dx2h * x2h, axis=-1, keepdims=True))
        dx2_ref[...] = dx2
        dx2b = dx2.astype(BF16)
        dmerged = _dot_nt(dx2b, wo_v)
        dwo_ref[...] += _dot_tn(merged, dx2b)
        dya = dmerged * sa
        dyb = dmerged * sb
        dug_ref[:, 0:d] = dya * ya * (1.0 - sa)
        dug_ref[:, d:2 * d] = dyb * yb * (1.0 - sb)
        dyab = dya.astype(BF16)
        dybb = dyb.astype(BF16)
        dma_ref[...] = _dot_nt(dyab, wa_v)
        dmb_ref[...] = _dot_nt(dybb, wb_v)
        dwa_ref[...] += _dot_tn(ma, dyab)
        dwb_ref[...] += _dot_tn(mb, dybb)

    row = lambda w: pl.BlockSpec((tm, w), lambda i: (i, 0))
    full = lambda a: pl.BlockSpec(a.shape, lambda i: (0, 0))
    fshape = lambda a: jax.ShapeDtypeStruct(a.shape, F32)
    return pl.pallas_call(
        body, name="merge_fwd_bwd", grid=(s // tm,),
        in_specs=[row(D_HALF), row(D_HALF), row(GATE_COLS), row(d), row(d), full(wa), full(wb), full(wo), full(fg)],
        out_specs=[row(d), row(D_HALF), row(D_HALF), row(GATE_COLS), full(wa), full(wb), full(wo), full(fg), full(fg)],
        out_shape=[jax.ShapeDtypeStruct((s, d), F32), jax.ShapeDtypeStruct((s, D_HALF), F32),
                   jax.ShapeDtypeStruct((s, D_HALF), F32), jax.ShapeDtypeStruct((s, GATE_COLS), F32),
                   fshape(wa), fshape(wb), fshape(wo), fshape(fg), fshape(fg)],
        compiler_params=_params("arbitrary"),
    )(mix_a, mix_b, u_g, x, tgt, wa, wb, wo, fg)


def _lora_weight(w_up, a_up):
    z = jnp.zeros((LORA, D_HALF), w_up.dtype)
    return jnp.concatenate([jnp.concatenate([w_up, z], axis=1), jnp.concatenate([z, a_up], axis=1)], axis=0)


def _device_grads(x, tgt, norm_g, w_a, w_b, w_g, shift_mu, w_up, w0, a_up, a0, k_k, k_a, r_k, lnx_w, lnx_b,
                  f_bias, q_norm_g, k_norm_g, w_out_a, w_out_b, w_out, final_norm_g):
    wl = _lora_weight(w_up, a_up)
    rk = r_k.reshape(1, D_HALF)
    fb = jnp.pad(f_bias, ((0, 0), (0, LANES - N_HEADS)))
    qg = jnp.tile(q_norm_g, (1, N_HEADS))
    kg = jnp.tile(k_norm_g, (1, N_HEADS))
    fg = final_norm_g.reshape(1, D_MODEL)

    h = _rmsnorm_in(x, norm_g)
    u_a = _matmul_nn(h, w_a, "inproj_rwkv")
    u_b = _matmul_nn(h, w_b, "inproj_fox")
    u_g = _matmul_nn(h, w_g, "inproj_gate")

    r, dec, k, v, av, bv, gate_a = _rwkv_prep(u_a, shift_mu, wl, w0, a0, k_k, k_a)
    y, st = _wkv_fwd(r, dec, k, av, bv, v)
    mix_a = _rwkv_post(y, r, k, v, gate_a, lnx_w, lnx_b, rk)

    q, kn, vb, cc, cr = _fox_prep(u_b, fb, qg, kg)
    o, mix_b, lse = _attn_fwd(q, kn, vb, cc, cr, u_b)

    dx2, dmix_a, dmix_b, du_g, dwa, dwb, dwo, dfg, loss_vec = _merge(
        mix_a, mix_b, u_g, x, tgt, w_out_a, w_out_b, w_out, fg)

    do, dgate_b = _fox_post_bwd(dmix_b, o, u_b)
    dd = _attn_bwd_rowdot(q, kn, vb, do, lse, cc, cr)
    dq, dk_att, dv_att, dcr = _attn_bwd(q, kn, vb, do, lse, dd, cc, cr)
    dcum = jnp.pad(dcr.T, ((0, 0), (0, LANES - N_HEADS)))
    du_b, dqg, dkg, dfb = _fox_prep_bwd(u_b, dq, dk_att, dv_att, dgate_b, dcum, fb, qg, kg)

    dy, dr_b, dk_b, dv_b, dgate_a, dlw, dlb, drk = _rwkv_post_bwd(dmix_a, y, r, k, v, gate_a, lnx_w, lnx_b, rk)
    dr_s, dw_s, dk_s, dv_s, da_s, db_s = _wkv_bwd(r, dec, k, av, bv, v, dy, st)
    du_a, dmu, dwl, dw0, da0, dkkw, dkaw = _rwkv_prep_bwd(
        u_a, (dr_s, dw_s, dk_s, dv_s, da_s, db_s, dr_b, dk_b, dv_b, dgate_a), shift_mu, wl, w0, a0, k_k, k_a)

    h_t = h.T
    dw_a = _matmul_tn_acc(h_t, du_a, "dw_rwkv")
    dw_b = _matmul_tn_acc(h_t, du_b, "dw_fox")
    dw_g = _matmul_tn_acc(h_t, du_g, "dw_gate")
    grad_x, dnorm_g = _inproj_bwd(du_a, du_b, du_g, w_a, w_b, w_g, x, dx2, norm_g)

    grads = dict(
        norm_g=dnorm_g, w_in=jnp.concatenate([dw_a, dw_b[:, :FOX_REAL], dw_g], axis=1), shift_mu=dmu,
        w_lora_up=dwl[:LORA, :D_HALF], w0=dw0, a_lora_up=dwl[LORA:, D_HALF:], a0=da0, k_k=dkkw, k_a=dkaw,
        r_k=drk.reshape(1, N_HEADS, HEAD), lnx_w=dlw, lnx_b=dlb, f_bias=dfb[:, :N_HEADS],
        q_norm_g=dqg.reshape(N_HEADS, HEAD).sum(axis=0, keepdims=True),
        k_norm_g=dkg.reshape(N_HEADS, HEAD).sum(axis=0, keepdims=True),
        w_out_a=dwa, w_out_b=dwb, w_out=dwo, final_norm_g=dfg.reshape(D_MODEL))
    return loss_vec, grad_x, grads


CHIP_FLIPS = ((1, 0), (0, 1), (1, 1))
ANY = pl.BlockSpec(memory_space=pl.ANY)


def _position():
    return lax.axis_index("x"), lax.axis_index("y"), lax.axis_index("c")


def _flip(v, f):
    return 1 - v if f else v


def _gather_shards(shards):
    n = len(shards)
    n_remote = len(CHIP_FLIPS) * n

    def body(*refs):
        ins, outs = refs[:n], refs[n:2 * n]
        send_sems, recv_sems, local_sems = refs[2 * n:]
        x, y, c = _position()
        me = 2 * x + y
        local = [pltpu.make_async_copy(ins[t], outs[t].at[me], local_sems.at[t]) for t in range(n)]
        for cp in local:
            cp.start()
        sends, recvs = [], []
        for f, (fx, fy) in enumerate(CHIP_FLIPS):
            px, py = _flip(x, fx), _flip(y, fy)
            for t in range(n):
                sems = dict(send_sem=send_sems.at[f * n + t], recv_sem=recv_sems.at[f * n + t],
                            device_id=(px, py, c), device_id_type=MESH)
                sends.append(pltpu.make_async_remote_copy(src_ref=ins[t], dst_ref=outs[t].at[me], **sems))
                recvs.append(pltpu.make_async_remote_copy(src_ref=ins[t], dst_ref=outs[t].at[2 * px + py], **sems))
        for cp in sends:
            cp.start()
        for cp in recvs:
            cp.wait_recv()
        for cp in sends:
            cp.wait_send()
        for cp in local:
            cp.wait()

    return pl.pallas_call(
        body, name="gather_weights", in_specs=[ANY] * n, out_specs=[ANY] * n,
        out_shape=[jax.ShapeDtypeStruct((N_CHIPS,) + a.shape, a.dtype) for a in shards],
        scratch_shapes=[pltpu.SemaphoreType.DMA((n_remote,)), pltpu.SemaphoreType.DMA((n_remote,)),
                        pltpu.SemaphoreType.DMA((n,))],
        compiler_params=pltpu.CompilerParams(has_side_effects=True),
    )(*shards)


def _scatter_partials(stacks):
    n = len(stacks)
    n_remote = len(CHIP_FLIPS) * n

    def body(*refs):
        ins, outs = refs[:n], refs[n:2 * n]
        send_sems, recv_sems = refs[2 * n:]
        x, y, c = _position()
        sends, recvs = [], []
        for f, (fx, fy) in enumerate(CHIP_FLIPS):
            px, py = _flip(x, fx), _flip(y, fy)
            for t in range(n):
                cp = pltpu.make_async_remote_copy(
                    src_ref=ins[t].at[2 * px + py], dst_ref=outs[t].at[f],
                    send_sem=send_sems.at[f * n + t], recv_sem=recv_sems.at[f * n + t],
                    device_id=(px, py, c), device_id_type=MESH)
                sends.append(cp)
        for cp in sends:
            cp.start()
        for cp in sends:
            cp.wait_recv()
        for cp in sends:
            cp.wait_send()

    return pl.pallas_call(
        body, name="scatter_partials", in_specs=[ANY] * n, out_specs=[ANY] * n,
        out_shape=[jax.ShapeDtypeStruct((len(CHIP_FLIPS),) + a.shape[1:], a.dtype) for a in stacks],
        scratch_shapes=[pltpu.SemaphoreType.DMA((n_remote,)), pltpu.SemaphoreType.DMA((n_remote,))],
        compiler_params=pltpu.CompilerParams(has_side_effects=True),
    )(*stacks)


def _swap_sibling(tensors):
    n = len(tensors)

    def body(*refs):
        ins, outs = refs[:n], refs[n:2 * n]
        send_sems, recv_sems = refs[2 * n:]
        x, y, c = _position()
        copies = [pltpu.make_async_remote_copy(
            src_ref=ins[t], dst_ref=outs[t], send_sem=send_sems.at[t], recv_sem=recv_sems.at[t],
            device_id=(x, y, 1 - c), device_id_type=MESH) for t in range(n)]
        for cp in copies:
            cp.start()
        for cp in copies:
            cp.wait_recv()
        for cp in copies:
            cp.wait_send()

    return pl.pallas_call(
        body, name="swap_sibling", in_specs=[ANY] * n, out_specs=[ANY] * n,
        out_shape=[jax.ShapeDtypeStruct(a.shape, a.dtype) for a in tensors],
        scratch_shapes=[pltpu.SemaphoreType.DMA((n,)), pltpu.SemaphoreType.DMA((n,))],
        compiler_params=pltpu.CompilerParams(has_side_effects=True),
    )(*tensors)


def _allreduce_small(slab):
    stages = 3

    def body(x_ref, o_ref, buf, send_sems, recv_sems):
        x, y, c = _position()
        peers = ((1 - x, y, c), (x, 1 - y, c), (x, y, 1 - c))
        o_ref[...] = x_ref[...]
        for k, peer in enumerate(peers):
            cp = pltpu.make_async_remote_copy(src_ref=o_ref, dst_ref=buf.at[k], send_sem=send_sems.at[k],
                                              recv_sem=recv_sems.at[k], device_id=peer, device_id_type=MESH)
            cp.start()
            cp.wait()
            o_ref[...] = o_ref[...] + buf[k]

    return pl.pallas_call(
        body, name="allreduce_small",
        in_specs=[pl.BlockSpec(memory_space=pltpu.VMEM)], out_specs=pl.BlockSpec(memory_space=pltpu.VMEM),
        out_shape=jax.ShapeDtypeStruct(slab.shape, slab.dtype),
        scratch_shapes=[pltpu.VMEM((stages,) + slab.shape, slab.dtype),
                        pltpu.SemaphoreType.DMA((stages,)), pltpu.SemaphoreType.DMA((stages,))],
        compiler_params=pltpu.CompilerParams(has_side_effects=True),
    )(slab)


def _row_tile(r):
    return min(r, 256)


def _sum4(stack, recv, me):
    _, r, c = stack.shape
    tr = _row_tile(r)

    def body(me_ref, own_ref, recv_ref, o_ref):
        o_ref[...] = ((own_ref[...] + recv_ref[0]) + recv_ref[1]) + recv_ref[2]

    return pl.pallas_call(
        body, name="sum_partials",
        grid_spec=pltpu.PrefetchScalarGridSpec(
            num_scalar_prefetch=1, grid=(r // tr,),
            in_specs=[pl.BlockSpec((None, tr, c), lambda i, me_ref: (me_ref[0], i, 0)),
                      pl.BlockSpec((len(CHIP_FLIPS), tr, c), lambda i, me_ref: (0, i, 0))],
            out_specs=pl.BlockSpec((tr, c), lambda i, me_ref: (i, 0))),
        out_shape=jax.ShapeDtypeStruct((r, c), F32), compiler_params=_params("parallel"),
    )(me, stack, recv)


def _adamw_math(w, g, m, v):
    m = ADAM_B1 * m + (1.0 - ADAM_B1) * g
    v = ADAM_B2 * v + (1.0 - ADAM_B2) * (g * g)
    m_hat = m / (1.0 - ADAM_B1 ** ADAM_STEP)
    v_hat = v / (1.0 - ADAM_B2 ** ADAM_STEP)
    delta = -ADAM_LR * (m_hat / (jnp.sqrt(v_hat) + ADAM_EPS) + ADAM_WD * w)
    return delta, m, v


def _adamw(w, m, v, g_parts, name):
    r, c = w.shape
    tr = _row_tile(r)
    n = len(g_parts)

    def body(*refs):
        w_ref, m_ref, v_ref = refs[:3]
        g_refs = refs[3:3 + n]
        g_out, d_out, m_out, v_out = refs[3 + n:]
        g = g_refs[0][...]
        for ref in g_refs[1:]:
            g = g + ref[...]
        g_out[...] = g
        d_out[...], m_out[...], v_out[...] = _adamw_math(w_ref[...], g, m_ref[...], v_ref[...])

    blk = pl.BlockSpec((tr, c), lambda i: (i, 0))
    return pl.pallas_call(
        body, name=name, grid=(r // tr,), in_specs=[blk] * (3 + n), out_specs=[blk] * 4,
        out_shape=[jax.ShapeDtypeStruct((r, c), F32)] * 4, compiler_params=_params("parallel"),
    )(w, m, v, *g_parts)


SHARDED = ("w_in", "w_lora_up", "a_lora_up", "w_out_a", "w_out_b", "w_out")
ROW_SHARDED = ("w_out",)
SMALL = ("norm_g", "shift_mu", "w0", "a0", "k_k", "k_a", "r_k", "lnx_w", "lnx_b", "f_bias", "q_norm_g", "k_norm_g",
         "final_norm_g")
WEIGHTS = ("norm_g", "w_in", "shift_mu", "w_lora_up", "w0", "a_lora_up", "a0", "k_k", "k_a", "r_k", "lnx_w", "lnx_b",
           "f_bias", "q_norm_g", "k_norm_g", "w_out_a", "w_out_b", "w_out", "final_norm_g")
SLAB_ROWS = 16
SLAB_COLS = SEC


def _to_slab(named, extra=None):
    rows = [jnp.pad(named[n].reshape(1, -1), ((0, 0), (0, SLAB_COLS - named[n].size))) for n in SMALL]
    if extra is not None:
        rows.append(jnp.pad(extra.reshape(1, -1), ((0, 0), (0, SLAB_COLS - extra.size))))
    rows.append(jnp.zeros((SLAB_ROWS - len(rows), SLAB_COLS), F32))
    return jnp.concatenate(rows, axis=0)


def _from_slab(slab, shapes):
    return {n: slab[i, :math.prod(shapes[n])].reshape(shapes[n]) for i, n in enumerate(SMALL)}


def _by_chip(g, name):
    if name in ROW_SHARDED:
        return g.reshape(N_CHIPS, g.shape[0] // N_CHIPS, g.shape[1])
    r, c = g.shape
    return g.reshape(r, N_CHIPS, c // N_CHIPS).transpose(1, 0, 2)


def _from_chips(stack, name):
    if name in ROW_SHARDED:
        return stack.reshape(-1, stack.shape[2])
    _, r, c = stack.shape
    return stack.transpose(1, 0, 2).reshape(r, N_CHIPS * c)


def kernel(x, norm_g, w_in, shift_mu, w_lora_up, w0, a_lora_up, a0, k_k, k_a, r_k, lnx_w, lnx_b, f_bias, q_norm_g, k_norm_g, w_out_a, w_out_b, w_out, final_norm_g, loss_target, m_norm_g, m_w_in, m_shift_mu, m_w_lora_up, m_w0, m_a_lora_up, m_a0, m_k_k, m_k_a, m_r_k, m_lnx_w, m_lnx_b, m_f_bias, m_q_norm_g, m_k_norm_g, m_w_out_a, m_w_out_b, m_w_out, m_final_norm_g, v_norm_g, v_w_in, v_shift_mu, v_w_lora_up, v_w0, v_a_lora_up, v_a0, v_k_k, v_k_a, v_r_k, v_lnx_w, v_lnx_b, v_f_bias, v_q_norm_g, v_k_norm_g, v_w_out_a, v_w_out_b, v_w_out, v_final_norm_g):
    w = dict(norm_g=norm_g, w_in=w_in, shift_mu=shift_mu, w_lora_up=w_lora_up, w0=w0, a_lora_up=a_lora_up, a0=a0,
             k_k=k_k, k_a=k_a, r_k=r_k, lnx_w=lnx_w, lnx_b=lnx_b, f_bias=f_bias, q_norm_g=q_norm_g,
             k_norm_g=k_norm_g, w_out_a=w_out_a, w_out_b=w_out_b, w_out=w_out, final_norm_g=final_norm_g)
    m = dict(norm_g=m_norm_g, w_in=m_w_in, shift_mu=m_shift_mu, w_lora_up=m_w_lora_up, w0=m_w0,
             a_lora_up=m_a_lora_up, a0=m_a0, k_k=m_k_k, k_a=m_k_a, r_k=m_r_k, lnx_w=m_lnx_w, lnx_b=m_lnx_b,
             f_bias=m_f_bias, q_norm_g=m_q_norm_g, k_norm_g=m_k_norm_g, w_out_a=m_w_out_a, w_out_b=m_w_out_b,
             w_out=m_w_out, final_norm_g=m_final_norm_g)
    v = dict(norm_g=v_norm_g, w_in=v_w_in, shift_mu=v_shift_mu, w_lora_up=v_w_lora_up, w0=v_w0,
             a_lora_up=v_a_lora_up, a0=v_a0, k_k=v_k_k, k_a=v_k_a, r_k=v_r_k, lnx_w=v_lnx_w, lnx_b=v_lnx_b,
             f_bias=v_f_bias, q_norm_g=v_q_norm_g, k_norm_g=v_k_norm_g, w_out_a=v_w_out_a, w_out_b=v_w_out_b,
             w_out=v_w_out, final_norm_g=v_final_norm_g)
    shapes = {n: w[n].shape for n in WEIGHTS}

    gathered = _gather_shards([w[n][0].astype(BF16) for n in SHARDED])
    full = {n: _from_chips(g, n) for n, g in zip(SHARDED, gathered)}
    w_full = full["w_in"]
    w_a = w_full[:, :RWKV_COLS]
    w_b = jnp.pad(w_full[:, RWKV_COLS:RWKV_COLS + FOX_REAL], ((0, 0), (0, SEC - FOX_REAL)))
    w_g = w_full[:, RWKV_COLS + FOX_REAL:]

    loss_vec, grad_x, grads = _device_grads(
        x[0], loss_target[0], norm_g, w_a, w_b, w_g, shift_mu, full["w_lora_up"], w0, full["a_lora_up"], a0, k_k, k_a,
        r_k, lnx_w, lnx_b, f_bias, q_norm_g, k_norm_g, full["w_out_a"], full["w_out_b"], full["w_out"], final_norm_g)

    total = _allreduce_small(_to_slab(grads, extra=loss_vec))
    loss = (0.5 / D_MODEL) * jnp.sum(total[len(SMALL)])
    slab_g, slab_d, slab_m, slab_v = _adamw(_to_slab(w), _to_slab(m), _to_slab(v), [total], "adamw_small")
    out_g, out_d, out_m, out_v = (_from_slab(s, shapes) for s in (total, slab_d, slab_m, slab_v))
    del slab_g

    xpos, ypos, _ = _position()
    me = (2 * xpos + ypos).astype(jnp.int32).reshape(1)
    stacks = [_by_chip(grads[n], n) for n in SHARDED]
    received = _scatter_partials(stacks)
    core_sums = [_sum4(s, r, me) for s, r in zip(stacks, received)]
    sibling_sums = _swap_sibling(core_sums)
    for n, mine, theirs in zip(SHARDED, core_sums, sibling_sums):
        g, d, m2, v2 = _adamw(w[n][0], m[n][0], v[n][0], [mine, theirs], "adamw_" + n)
        out_g[n], out_d[n], out_m[n], out_v[n] = (a.reshape(shapes[n]) for a in (g, d, m2, v2))

    return (loss, grad_x.reshape(x.shape), *[out_g[n] for n in WEIGHTS], *[out_d[n] for n in WEIGHTS],
            *[out_m[n] for n in WEIGHTS], *[out_v[n] for n in WEIGHTS])
```

```python
import functools
import math

import jax
import jax.numpy as jnp
from jax import lax
from jax.experimental import pallas as pl
from jax.experimental.pallas import tpu as pltpu

F32 = jnp.float32
BF16 = jnp.bfloat16

D_MODEL = 1024
D_HALF = 512
HEAD = 64
N_HEADS = 8
LORA = 64
RWKV_COLS = 2176
FOX_REAL = 2056
SEC = 2176
GATE_COLS = 2048
IN_COLS = 6280
N_CHIPS = 4
SHARD_COLS = IN_COLS // N_CHIPS
RMS_EPS = 1e-6
LNX_EPS = 64e-5
ATT_SCALE = HEAD ** -0.5
NEG = -1e30

ADAM_LR = 0.001
ADAM_B1 = 0.9
ADAM_B2 = 0.999
ADAM_EPS = 1e-08
ADAM_WD = 0.01
ADAM_STEP = 10

LANES = 128
SUBLANES = 8
VMEM_LIMIT = 56 * 1024 * 1024
MESH = pl.DeviceIdType.MESH


def _params(*sem):
    return pltpu.CompilerParams(dimension_semantics=sem if sem else None, vmem_limit_bytes=VMEM_LIMIT)


def _sigmoid(x):
    return 1.0 / (1.0 + jnp.exp(-x))


def _log_sigmoid(x):
    return jnp.minimum(x, 0.0) - jnp.log(1.0 + jnp.exp(-jnp.abs(x)))


def _head_ones():
    r = lax.broadcasted_iota(jnp.int32, (LANES, LANES), 0) >> 6
    c = lax.broadcasted_iota(jnp.int32, (LANES, LANES), 1) >> 6
    return (r == c).astype(BF16)


def _split3(x):
    hi = x.astype(BF16)
    r1 = x - hi.astype(F32)
    mid = r1.astype(BF16)
    lo = (r1 - mid.astype(F32)).astype(BF16)
    return hi, mid, lo


def _exact_dot(x, ones_bf16, ones_first=False):
    out = None
    for piece in _split3(x):
        if ones_first:
            t = jnp.dot(ones_bf16, piece, preferred_element_type=F32)
        else:
            t = jnp.dot(piece, ones_bf16, preferred_element_type=F32)
        out = t if out is None else out + t
    return out


def _head_sum(x, bd):
    n = x.shape[1] // LANES
    parts = [_exact_dot(x[:, i * LANES:(i + 1) * LANES], bd) for i in range(n)]
    return parts[0] if n == 1 else jnp.concatenate(parts, axis=1)


def _dot_nt(a, b):
    return lax.dot_general(a, b, (((1,), (1,)), ((), ())), preferred_element_type=F32)


def _dot_tn(a, b):
    return lax.dot_general(a, b, (((0,), (0,)), ((), ())), preferred_element_type=F32)


def _colsum(x):
    return jnp.sum(x, axis=0, keepdims=True)


def _rmsnorm_in(x, g, tm=512):
    s, d = x.shape

    def body(x_ref, g_ref, h_ref):
        xv = x_ref[...]
        r = lax.rsqrt(jnp.mean(xv * xv, axis=-1, keepdims=True) + RMS_EPS)
        h_ref[...] = (xv * r * g_ref[...]).astype(BF16)

    return pl.pallas_call(
        body, name="rmsnorm_in", grid=(s // tm,),
        in_specs=[pl.BlockSpec((tm, d), lambda i: (i, 0)), pl.BlockSpec((1, d), lambda i: (0, 0))],
        out_specs=pl.BlockSpec((tm, d), lambda i: (i, 0)),
        out_shape=jax.ShapeDtypeStruct((s, d), BF16), compiler_params=_params("parallel"),
    )(x, g)


def _matmul_nn(a, b, name, tm=512):
    m, k = a.shape
    n = b.shape[1]

    def body(a_ref, b_ref, o_ref):
        o_ref[...] = jnp.dot(a_ref[...], b_ref[...], preferred_element_type=F32)

    return pl.pallas_call(
        body, name=name, grid=(m // tm,),
        in_specs=[pl.BlockSpec((tm, k), lambda i: (i, 0)), pl.BlockSpec((k, n), lambda i: (0, 0))],
        out_specs=pl.BlockSpec((tm, n), lambda i: (i, 0)),
        out_shape=jax.ShapeDtypeStruct((m, n), F32), compiler_params=_params("parallel"),
    )(a, b)


def _matmul_tn_acc(at, b, name, tk=512):
    m, k = at.shape
    n = b.shape[1]

    def body(a_ref, b_ref, o_ref):
        j = pl.program_id(0)

        @pl.when(j == 0)
        def _():
            o_ref[...] = jnp.zeros_like(o_ref)

        o_ref[...] += jnp.dot(a_ref[...], b_ref[...].astype(BF16), preferred_element_type=F32)

    return pl.pallas_call(
        body, name=name, grid=(k // tk,),
        in_specs=[pl.BlockSpec((m, tk), lambda j: (0, j)), pl.BlockSpec((tk, n), lambda j: (j, 0))],
        out_specs=pl.BlockSpec((m, n), lambda j: (0, 0)),
        out_shape=jax.ShapeDtypeStruct((m, n), F32), compiler_params=_params("arbitrary"),
    )(at, b)


def _inproj_bwd(du_a, du_b, du_g, w_a, w_b, w_g, x, dx2, g, tm=256):
    s, d = x.shape

    def body(da_ref, db_ref, dg_ref, wa_ref, wb_ref, wg_ref, x_ref, dx2_ref, g_ref, gx_ref, gg_ref):
        i = pl.program_id(0)

        @pl.when(i == 0)
        def _():
            gg_ref[...] = jnp.zeros_like(gg_ref)

        dh = _dot_nt(da_ref[...].astype(BF16), wa_ref[...])
        dh += _dot_nt(db_ref[...].astype(BF16), wb_ref[...])
        dh += _dot_nt(dg_ref[...].astype(BF16), wg_ref[...])
        xv = x_ref[...]
        r = lax.rsqrt(jnp.mean(xv * xv, axis=-1, keepdims=True) + RMS_EPS)
        xh = xv * r
        gg_ref[...] += _colsum(dh * xh)
        dxh = dh * g_ref[...]
        gx_ref[...] = dx2_ref[...] + r * (dxh - xh * jnp.mean(dxh * xh, axis=-1, keepdims=True))

    row = lambda w: pl.BlockSpec((tm, w), lambda i: (i, 0))
    full = lambda a: pl.BlockSpec(a.shape, lambda i: (0, 0))
    return pl.pallas_call(
        body, name="inproj_bwd", grid=(s // tm,),
        in_specs=[row(SEC), row(SEC), row(GATE_COLS), full(w_a), full(w_b), full(w_g), row(d), row(d), full(g)],
        out_specs=[row(d), pl.BlockSpec((1, d), lambda i: (0, 0))],
        out_shape=[jax.ShapeDtypeStruct((s, d), F32), jax.ShapeDtypeStruct((1, d), F32)],
        compiler_params=_params("arbitrary"),
    )(du_a, du_b, du_g, w_a, w_b, w_g, x, dx2, g)


def _rwkv_elementwise(ua, prev_row, first, mu, wl, w0, a0, kkw, kaw, bd):
    tm = ua.shape[0]
    rows = lax.broadcasted_iota(jnp.int32, (tm, 1), 0)
    prev = jnp.where(first, jnp.zeros_like(prev_row), prev_row)
    shifted = jnp.where(rows == 0, prev, pltpu.roll(ua, 1, 0))
    delta = shifted - ua
    us = ua + delta * mu
    r = us[:, 0:512]
    k0 = us[:, 512:1024]
    v = us[:, 1024:1536]
    lo = us[:, 1536:1664]
    gate = us[:, 1664:2176]
    lane = lax.broadcasted_iota(jnp.int32, (1, LANES), 1)
    th = jnp.tanh(lo)
    lin = jnp.where(lane < LORA, th, lo)
    ll = jnp.dot(lin.astype(BF16), wl, preferred_element_type=F32)
    sz = _sigmoid(w0 + ll[:, :512])
    e = sz * math.exp(-0.5)
    dec = jnp.exp(-e)
    a = _sigmoid(a0 + ll[:, 512:])
    kk0 = k0 * kkw
    ss = _head_sum(kk0 * kk0, bd)
    nrm = jnp.maximum(jnp.sqrt(ss), 1e-12)
    kk = kk0 / nrm
    k = k0 * (1.0 + (a - 1.0) * kaw)
    return dict(delta=delta, us=us, r=r, k0=k0, v=v, lo=lo, gate=gate, th=th, lin=lin, sz=sz, e=e, dec=dec,
                a=a, kk0=kk0, ss=ss, nrm=nrm, kk=kk, k=k)


def _rwkv_prep(u_a, mu, wl, w0, a0, kkw, kaw, tm=256):
    s = u_a.shape[0]

    def body(ua_ref, prev_ref, mu_ref, wl_ref, w0_ref, a0_ref, kkw_ref, kaw_ref,
             r_ref, w_ref, k_ref, v_ref, a_ref, b_ref, g_ref):
        i = pl.program_id(0)
        f = _rwkv_elementwise(ua_ref[...], prev_ref[7:8, :], i == 0, mu_ref[...], wl_ref[...], w0_ref[...],
                              a0_ref[...], kkw_ref[...], kaw_ref[...], _head_ones())
        r_ref[...] = f["r"]
        w_ref[...] = f["dec"]
        k_ref[...] = f["k"]
        v_ref[...] = f["v"]
        a_ref[...] = -f["kk"]
        b_ref[...] = f["kk"] * f["a"]
        g_ref[...] = f["gate"]

    vec = lambda w: pl.BlockSpec((1, w), lambda i: (0, 0))
    out = pl.BlockSpec((tm, D_HALF), lambda i: (i, 0))
    return pl.pallas_call(
        body, name="rwkv_prep", grid=(s // tm,),
        in_specs=[pl.BlockSpec((tm, SEC), lambda i: (i, 0)),
                  pl.BlockSpec((8, SEC), lambda i: (jnp.maximum(i * (tm // 8) - 1, 0), 0)),
                  vec(SEC), pl.BlockSpec((LANES, 2 * D_HALF), lambda i: (0, 0)),
                  vec(D_HALF), vec(D_HALF), vec(D_HALF), vec(D_HALF)],
        out_specs=[out] * 7,
        out_shape=[jax.ShapeDtypeStruct((s, D_HALF), F32)] * 7,
        compiler_params=_params("parallel"),
    )(u_a, u_a, mu, wl, w0, a0, kkw, kaw)


SCAN_TB = 64
N_PAIRS = 4


def _pair_sum(x, left):
    s_l = jnp.sum(jnp.where(left, x, 0.0), axis=1, keepdims=True)
    s_r = jnp.sum(jnp.where(left, 0.0, x), axis=1, keepdims=True)
    return jnp.where(left, s_l, s_r)


def _quad_consts():
    lane = lax.broadcasted_iota(jnp.int32, (HEAD, 2 * LANES), 1)
    rowi = lax.broadcasted_iota(jnp.int32, (HEAD, 2 * LANES), 0)
    diag2 = rowi == (lane & (HEAD - 1))
    r = lax.broadcasted_iota(jnp.int32, (2 * LANES, 2 * LANES), 0) >> 6
    c = lax.broadcasted_iota(jnp.int32, (2 * LANES, 2 * LANES), 1) >> 6
    return diag2, (r == c).astype(BF16)


def _rows_to_columns(x8, diag2, bd2):
    hi = x8.astype(BF16).astype(F32)
    lo = x8 - hi
    lhs = jnp.concatenate([jnp.where(diag2, piece[i:i + 1], 0.0).astype(BF16)
                           for piece in (hi, lo) for i in range(SUBLANES)], axis=0)
    res = jnp.dot(lhs, bd2, preferred_element_type=F32)
    half = SUBLANES * HEAD
    return [res[i * HEAD:(i + 1) * HEAD] + res[half + i * HEAD:half + (i + 1) * HEAD] for i in range(SUBLANES)]


def _left_half():
    return lax.broadcasted_iota(jnp.int32, (HEAD, LANES), 1) < HEAD


def _wkv_fwd(r, w, k, a, b, v):
    s = r.shape[0]
    tb = SCAN_TB

    def body(r_ref, w_ref, k_ref, a_ref, b_ref, v_ref, y_ref, st_ref, state, qbuf):
        g = pl.program_id(0)

        @pl.when(g == 0)
        def _():
            state[...] = jnp.zeros_like(state)

        left = _left_half()
        diag2, bd2 = _quad_consts()
        sub_row = lax.broadcasted_iota(jnp.int32, (SUBLANES, 2 * LANES), 0)
        half = SUBLANES * HEAD

        def group(q, carry):
            rows8 = pl.ds(pl.multiple_of(q * SUBLANES, SUBLANES), SUBLANES)
            a8, w8, b8, k8, r8, v8 = (x[rows8, :] for x in (a_ref, w_ref, b_ref, k_ref, r_ref, v_ref))
            vb = [_rows_to_columns(v8[:, g2 * 2 * LANES:(g2 + 1) * 2 * LANES], diag2, bd2) for g2 in range(2)]
            sp = [state[p] for p in range(N_PAIRS)]
            for i in range(SUBLANES):
                row = slice(i, i + 1)
                for p in range(N_PAIRS):
                    st_ref[q * SUBLANES + i, p] = sp[p]
                sa = [_pair_sum(sp[p] * a8[row, p * LANES:(p + 1) * LANES], left) for p in range(N_PAIRS)]
                for p in range(N_PAIRS):
                    lanes = slice(p * LANES, (p + 1) * LANES)
                    inner = slice((p % 2) * LANES, (p % 2 + 1) * LANES)
                    sp[p] = sp[p] * w8[row, lanes] + sa[p] * b8[row, lanes] + vb[p // 2][i][:, inner] * k8[row, lanes]
                    qv = sp[p] * r8[row, lanes]
                    hi = qv.astype(BF16)
                    qbuf[p // 2, i * HEAD:(i + 1) * HEAD, inner] = hi
                    qbuf[p // 2, half + i * HEAD:half + (i + 1) * HEAD, inner] = (qv - hi.astype(F32)).astype(BF16)
            for p in range(N_PAIRS):
                state[p] = sp[p]
            for g2 in range(2):
                res = jnp.dot(qbuf[g2], bd2, preferred_element_type=F32)
                y8 = jnp.zeros((SUBLANES, 2 * LANES), F32)
                for i in range(SUBLANES):
                    yb = res[i * HEAD:(i + 1) * HEAD] + res[half + i * HEAD:half + (i + 1) * HEAD]
                    y8 = jnp.where(sub_row == i, _colsum(jnp.where(diag2, yb, 0.0)), y8)
                y_ref[rows8, g2 * 2 * LANES:(g2 + 1) * 2 * LANES] = y8
            return carry

        lax.fori_loop(0, tb // SUBLANES, group, 0)

    rows = pl.BlockSpec((tb, D_HALF), lambda g: (g, 0))
    return pl.pallas_call(
        body, name="wkv_fwd", grid=(s // tb,),
        in_specs=[rows] * 6,
        out_specs=[rows, pl.BlockSpec((tb, N_PAIRS, HEAD, LANES), lambda g: (g, 0, 0, 0))],
        out_shape=[jax.ShapeDtypeStruct((s, D_HALF), F32),
                   jax.ShapeDtypeStruct((s, N_PAIRS, HEAD, LANES), F32)],
        scratch_shapes=[pltpu.VMEM((N_PAIRS, HEAD, LANES), F32),
                        pltpu.VMEM((2, 2 * SUBLANES * HEAD, 2 * LANES), BF16)],
        compiler_params=_params("arbitrary"),
    )(r, w, k, a, b, v)


def _wkv_bwd(r, w, k, a, b, v, dy, st):
    s = r.shape[0]
    tb = SCAN_TB
    nb = s // tb

    def body(r_ref, w_ref, k_ref, a_ref, b_ref, v_ref, dy_ref, st_ref,
             dr_ref, dw_ref, dk_ref, dv_ref, da_ref, db_ref, dstate, qbuf):
        g = pl.program_id(0)

        @pl.when(g == 0)
        def _():
            dstate[...] = jnp.zeros_like(dstate)

        left = _left_half()
        diag2, bd2 = _quad_consts()
        sub_row = lax.broadcasted_iota(jnp.int32, (SUBLANES, LANES), 0)
        sub_row2 = lax.broadcasted_iota(jnp.int32, (SUBLANES, 2 * LANES), 0)
        half = SUBLANES * HEAD
        row_refs = (dr_ref, dw_ref, dk_ref, da_ref, db_ref)

        def group(qq, carry):
            q = tb // SUBLANES - 1 - qq
            rows8 = pl.ds(pl.multiple_of(q * SUBLANES, SUBLANES), SUBLANES)
            a8, w8, b8, k8, r8, v8, dy8 = (x[rows8, :] for x in (a_ref, w_ref, b_ref, k_ref, r_ref, v_ref, dy_ref))
            quads = [slice(g2 * 2 * LANES, (g2 + 1) * 2 * LANES) for g2 in range(2)]
            vb = [_rows_to_columns(v8[:, qd], diag2, bd2) for qd in quads]
            dyb = [_rows_to_columns(dy8[:, qd], diag2, bd2) for qd in quads]
            dsp = [dstate[p] for p in range(N_PAIRS)]
            outs = [[jnp.zeros((SUBLANES, LANES), F32) for _ in row_refs] for _ in range(N_PAIRS)]
            for i in reversed(range(SUBLANES)):
                row = slice(i, i + 1)
                pl_ = [slice(p * LANES, (p + 1) * LANES) for p in range(N_PAIRS)]
                inner = [slice((p % 2) * LANES, (p % 2 + 1) * LANES) for p in range(N_PAIRS)]
                sp = [st_ref[q * SUBLANES + i, p] for p in range(N_PAIRS)]
                dyt = [dyb[p // 2][i][:, inner[p]] for p in range(N_PAIRS)]
                ds = [dsp[p] + dyt[p] * r8[row, pl_[p]] for p in range(N_PAIRS)]
                dsa = [_pair_sum(ds[p] * b8[row, pl_[p]], left) for p in range(N_PAIRS)]
                sa = [_pair_sum(sp[p] * a8[row, pl_[p]], left) for p in range(N_PAIRS)]
                for p in range(N_PAIRS):
                    ar, wr, br, kr = (x[row, pl_[p]] for x in (a8, w8, b8, k8))
                    vt = vb[p // 2][i][:, inner[p]]
                    dsp[p] = ds[p] * wr + dsa[p] * ar
                    sn = sp[p] * wr + sa[p] * br + vt * kr
                    new = (_colsum(sn * dyt[p]), _colsum(ds[p] * sp[p]), _colsum(ds[p] * vt),
                           _colsum(sp[p] * dsa[p]), _colsum(ds[p] * sa[p]))
                    outs[p] = [jnp.where(sub_row == i, n, o) for n, o in zip(new, outs[p])]
                    xv = ds[p] * kr
                    hi = xv.astype(BF16)
                    qbuf[p // 2, i * HEAD:(i + 1) * HEAD, inner[p]] = hi
                    qbuf[p // 2, half + i * HEAD:half + (i + 1) * HEAD, inner[p]] = (xv - hi.astype(F32)).astype(BF16)
            for p in range(N_PAIRS):
                dstate[p] = dsp[p]
                for ref, o in zip(row_refs, outs[p]):
                    ref[rows8, p * LANES:(p + 1) * LANES] = o
            for g2 in range(2):
                res = jnp.dot(qbuf[g2], bd2, preferred_element_type=F32)
                dv8 = jnp.zeros((SUBLANES, 2 * LANES), F32)
                for i in range(SUBLANES):
                    t = res[i * HEAD:(i + 1) * HEAD] + res[half + i * HEAD:half + (i + 1) * HEAD]
                    dv8 = jnp.where(sub_row2 == i, _colsum(jnp.where(diag2, t, 0.0)), dv8)
                dv_ref[rows8, quads[g2]] = dv8
            return carry

        lax.fori_loop(0, tb // SUBLANES, group, 0)

    rows = pl.BlockSpec((tb, D_HALF), lambda g: (nb - 1 - g, 0))
    return pl.pallas_call(
        body, name="wkv_bwd", grid=(nb,),
        in_specs=[rows] * 7 + [pl.BlockSpec((tb, N_PAIRS, HEAD, LANES), lambda g: (nb - 1 - g, 0, 0, 0))],
        out_specs=[rows] * 6,
        out_shape=[jax.ShapeDtypeStruct((s, D_HALF), F32)] * 6,
        scratch_shapes=[pltpu.VMEM((N_PAIRS, HEAD, LANES), F32),
                        pltpu.VMEM((2, 2 * SUBLANES * HEAD, 2 * LANES), BF16)],
        compiler_params=_params("arbitrary"),
    )(r, w, k, a, b, v, dy, st)


def _rwkv_post_math(y, r, k, v, gate, lw, lb, rk, bd):
    mean = _head_sum(y, bd) * (1.0 / HEAD)
    yc = y - mean
    var = _head_sum(yc * yc, bd) * (1.0 / HEAD)
    rstd = lax.rsqrt(var + LNX_EPS)
    yn = yc * rstd
    rkk = _head_sum(r * k * rk, bd)
    sg = _sigmoid(gate)
    pre = yn * lw + lb + rkk * v
    return yn, rstd, rkk, sg, pre


def _rwkv_post(y, r, k, v, gate, lw, lb, rk, tm=256):
    s = y.shape[0]

    def body(y_ref, r_ref, k_ref, v_ref, g_ref, lw_ref, lb_ref, rk_ref, o_ref):
        gate_v = g_ref[...]
        _, _, _, sg, pre = _rwkv_post_math(y_ref[...], r_ref[...], k_ref[...], v_ref[...], gate_v,
                                           lw_ref[...], lb_ref[...], rk_ref[...], _head_ones())
        o_ref[...] = pre * (gate_v * sg)

    blk = pl.BlockSpec((tm, D_HALF), lambda i: (i, 0))
    vec = pl.BlockSpec((1, D_HALF), lambda i: (0, 0))
    return pl.pallas_call(
        body, name="rwkv_post", grid=(s // tm,),
        in_specs=[blk] * 5 + [vec] * 3, out_specs=blk,
        out_shape=jax.ShapeDtypeStruct((s, D_HALF), F32), compiler_params=_params("parallel"),
    )(y, r, k, v, gate, lw, lb, rk)


def _rwkv_post_bwd(dmix, y, r, k, v, gate, lw, lb, rk, tm=256):
    s = y.shape[0]

    def body(dm_ref, y_ref, r_ref, k_ref, v_ref, g_ref, lw_ref, lb_ref, rk_ref,
             dy_ref, dr_ref, dk_ref, dv_ref, dg_ref, dlw_ref, dlb_ref, drk_ref):
        i = pl.program_id(0)

        @pl.when(i == 0)
        def _():
            dlw_ref[...] = jnp.zeros_like(dlw_ref)
            dlb_ref[...] = jnp.zeros_like(dlb_ref)
            drk_ref[...] = jnp.zeros_like(drk_ref)

        bd = _head_ones()
        rv, kv, vv, gate_v, lw_v, rk_v = r_ref[...], k_ref[...], v_ref[...], g_ref[...], lw_ref[...], rk_ref[...]
        yn, rstd, rkk, sg, pre = _rwkv_post_math(y_ref[...], rv, kv, vv, gate_v, lw_v, lb_ref[...], rk_v, bd)
        dm = dm_ref[...]
        dg_ref[...] = dm * pre * (sg * (1.0 + gate_v * (1.0 - sg)))
        dpre = dm * (gate_v * sg)
        dlw_ref[...] += _colsum(dpre * yn)
        dlb_ref[...] += _colsum(dpre)
        dyn = dpre * lw_v
        m1 = _head_sum(dyn, bd) * (1.0 / HEAD)
        m2 = _head_sum(dyn * yn, bd) * (1.0 / HEAD)
        dy_ref[...] = rstd * (dyn - m1 - yn * m2)
        dv_ref[...] = dpre * rkk
        drkk = _head_sum(dpre * vv, bd)
        dr_ref[...] = drkk * kv * rk_v
        dk_ref[...] = drkk * rv * rk_v
        drk_ref[...] += _colsum(drkk * rv * kv)

    blk = pl.BlockSpec((tm, D_HALF), lambda i: (i, 0))
    vec = pl.BlockSpec((1, D_HALF), lambda i: (0, 0))
    return pl.pallas_call(
        body, name="rwkv_post_bwd", grid=(s // tm,),
        in_specs=[blk] * 6 + [vec] * 3, out_specs=[blk] * 5 + [vec] * 3,
        out_shape=[jax.ShapeDtypeStruct((s, D_HALF), F32)] * 5 + [jax.ShapeDtypeStruct((1, D_HALF), F32)] * 3,
        compiler_params=_params("arbitrary"),
    )(dmix, y, r, k, v, gate, lw, lb, rk)


def _rwkv_prep_bwd(u_a, grads, mu, wl, w0, a0, kkw, kaw, tm=256):
    s = u_a.shape[0]
    nb = s // tm

    def body(ua_ref, prev_ref, drs_ref, dws_ref, dks_ref, dvs_ref, das_ref, dbs_ref, drb_ref, dkb_ref, dvb_ref,
             dgt_ref, mu_ref, wl_ref, w0_ref, a0_ref, kkw_ref, kaw_ref,
             du_ref, dmu_ref, dwl_ref, dw0_ref, da0_ref, dkkw_ref, dkaw_ref, carry):
        i = pl.program_id(0)

        @pl.when(i == 0)
        def _():
            carry[...] = jnp.zeros_like(carry)
            for ref in (dmu_ref, dwl_ref, dw0_ref, da0_ref, dkkw_ref, dkaw_ref):
                ref[...] = jnp.zeros_like(ref)

        bd = _head_ones()
        mu_v, wl_v, kkw_v, kaw_v = mu_ref[...], wl_ref[...], kkw_ref[...], kaw_ref[...]
        f = _rwkv_elementwise(ua_ref[...], prev_ref[7:8, :], i == nb - 1, mu_v, wl_v, w0_ref[...],
                              a0_ref[...], kkw_v, kaw_v, bd)
        a, kk, k0 = f["a"], f["kk"], f["k0"]
        dk = dks_ref[...] + dkb_ref[...]
        dbs = dbs_ref[...]
        dkk = dbs * a - das_ref[...]
        da = dbs * kk + dk * k0 * kaw_v
        dk0 = dk * (1.0 + (a - 1.0) * kaw_v)
        dkaw_ref[...] += _colsum(dk * k0 * (a - 1.0))
        inv = 1.0 / f["nrm"]
        proj = _head_sum(dkk * kk, bd)
        dkk0 = jnp.where(f["ss"] > 1e-24, (dkk - kk * proj) * inv, dkk * inv)
        dk0 = dk0 + dkk0 * kkw_v
        dkkw_ref[...] += _colsum(dkk0 * k0)
        dza = da * a * (1.0 - a)
        da0_ref[...] += _colsum(dza)
        dz = -dws_ref[...] * f["dec"] * f["e"] * (1.0 - f["sz"])
        dw0_ref[...] += _colsum(dz)
        dll = jnp.concatenate([dz, dza], axis=1).astype(BF16)
        dwl_ref[...] += _dot_tn(f["lin"].astype(BF16), dll)
        dlin = _dot_nt(dll, wl_v)
        lane = lax.broadcasted_iota(jnp.int32, (1, LANES), 1)
        th = f["th"]
        dlo = jnp.where(lane < LORA, dlin * (1.0 - th * th), dlin)
        dus = jnp.concatenate([drs_ref[...] + drb_ref[...], dk0, dvs_ref[...] + dvb_ref[...], dlo, dgt_ref[...]],
                              axis=1)
        dmu_ref[...] += _colsum(dus * f["delta"])
        g1 = dus * mu_v
        rows = lax.broadcasted_iota(jnp.int32, (tm, 1), 0)
        up = jnp.where(rows == tm - 1, carry[...], pltpu.roll(g1, tm - 1, 0))
        du_ref[...] = dus - g1 + up
        carry[...] = g1[0:1, :]

    rev = lambda w: pl.BlockSpec((tm, w), lambda i: (nb - 1 - i, 0))
    vec = lambda w: pl.BlockSpec((1, w), lambda i: (0, 0))
    wl_spec = pl.BlockSpec((LANES, 2 * D_HALF), lambda i: (0, 0))
    return pl.pallas_call(
        body, name="rwkv_prep_bwd", grid=(nb,),
        in_specs=[rev(SEC), pl.BlockSpec((8, SEC), lambda i: (jnp.maximum((nb - 1 - i) * (tm // 8) - 1, 0), 0))]
                 + [rev(D_HALF)] * 10 + [vec(SEC), wl_spec] + [vec(D_HALF)] * 4,
        out_specs=[rev(SEC), vec(SEC), wl_spec] + [vec(D_HALF)] * 4,
        out_shape=[jax.ShapeDtypeStruct((s, SEC), F32), jax.ShapeDtypeStruct((1, SEC), F32),
                   jax.ShapeDtypeStruct((LANES, 2 * D_HALF), F32)] + [jax.ShapeDtypeStruct((1, D_HALF), F32)] * 4,
        scratch_shapes=[pltpu.VMEM((1, SEC), F32)],
        compiler_params=_params("arbitrary"),
    )(u_a, u_a, *grads, mu, wl, w0, a0, kkw, kaw)


def _tri(tm, lower):
    r = lax.broadcasted_iota(jnp.int32, (tm, tm), 0)
    c = lax.broadcasted_iota(jnp.int32, (tm, tm), 1)
    return ((r >= c) if lower else (r <= c)).astype(BF16)


def _head_rms(x, g, bd):
    rinv = lax.rsqrt(_head_sum(x * x, bd) * (1.0 / HEAD) + RMS_EPS)
    xh = x * rinv
    return xh, rinv, xh * g


def _fox_prep(u_b, fb, qg, kg, tm=256):
    s = u_b.shape[0]

    def body(ub_ref, fb_ref, qg_ref, kg_ref, q_ref, k_ref, v_ref, cc_ref, cr_ref, carry):
        i = pl.program_id(0)

        @pl.when(i == 0)
        def _():
            carry[...] = jnp.zeros_like(carry)

        bd = _head_ones()
        _, _, qn = _head_rms(ub_ref[:, 0:512], qg_ref[...], bd)
        _, _, kn = _head_rms(ub_ref[:, 512:1024], kg_ref[...], bd)
        q_ref[...] = (qn * ATT_SCALE).astype(BF16)
        k_ref[...] = kn.astype(BF16)
        v_ref[...] = ub_ref[:, 1024:1536].astype(BF16)
        lane = lax.broadcasted_iota(jnp.int32, (1, LANES), 1)
        logf = jnp.where(lane < N_HEADS, _log_sigmoid(ub_ref[:, 2048:2176] + fb_ref[...]), 0.0)
        cum = _exact_dot(logf, _tri(tm, True), ones_first=True) + carry[...]
        cc_ref[...] = cum
        cr_ref[...] = jnp.transpose(cum)[0:N_HEADS, :]
        carry[...] = cum[tm - 1:tm, :]

    blk = pl.BlockSpec((tm, D_HALF), lambda i: (i, 0))
    return pl.pallas_call(
        body, name="fox_prep", grid=(s // tm,),
        in_specs=[pl.BlockSpec((tm, SEC), lambda i: (i, 0)), pl.BlockSpec((1, LANES), lambda i: (0, 0)),
                  pl.BlockSpec((1, D_HALF), lambda i: (0, 0)), pl.BlockSpec((1, D_HALF), lambda i: (0, 0))],
        out_specs=[blk, blk, blk, pl.BlockSpec((tm, LANES), lambda i: (i, 0)),
                   pl.BlockSpec((N_HEADS, tm), lambda i: (0, i))],
        out_shape=[jax.ShapeDtypeStruct((s, D_HALF), BF16)] * 3
                  + [jax.ShapeDtypeStruct((s, LANES), F32), jax.ShapeDtypeStruct((N_HEADS, s), F32)],
        scratch_shapes=[pltpu.VMEM((1, LANES), F32)],
        compiler_params=_params("arbitrary"),
    )(u_b, fb, qg, kg)


ATT_T = 256


def _attn_fwd(q, k, v, cc, cr, u_b):
    s = q.shape[0]
    t = ATT_T
    nblk = s // t

    def body(q_ref, k_ref, v_ref, cc_ref, cr_ref, g_ref, o_ref, mix_ref, lse_ref, m_sc, l_sc, acc_sc):
        i = pl.program_id(0)
        j = pl.program_id(1)

        @pl.when(j == 0)
        def _():
            m_sc[...] = jnp.full_like(m_sc, NEG)
            l_sc[...] = jnp.zeros_like(l_sc)
            acc_sc[...] = jnp.zeros_like(acc_sc)

        @pl.when(j <= i)
        def _():
            row = i * t + lax.broadcasted_iota(jnp.int32, (t, t), 0)
            col = j * t + lax.broadcasted_iota(jnp.int32, (t, t), 1)
            causal = row >= col
            left = lax.broadcasted_iota(jnp.int32, (1, LANES), 1) < HEAD
            for p in range(N_PAIRS):
                lanes = slice(p * LANES, (p + 1) * LANES)
                q2, k2, v2 = q_ref[:, lanes], k_ref[:, lanes], v_ref[:, lanes]
                acc2 = acc_sc[:, lanes]
                for e in range(2):
                    h = 2 * p + e
                    msk = left if e == 0 else jnp.logical_not(left)
                    sc = _dot_nt(jnp.where(msk, q2, jnp.zeros_like(q2)), k2)
                    sc = sc + (cc_ref[:, h:h + 1] - cr_ref[h:h + 1, :])
                    sc = jnp.where(causal, sc, NEG)
                    m_prev = m_sc[h]
                    m_new = jnp.maximum(m_prev, jnp.max(sc, axis=1, keepdims=True))
                    alpha = jnp.exp(m_prev - m_new)
                    pm = jnp.exp(sc - m_new)
                    l_sc[h] = alpha * l_sc[h] + jnp.sum(pm, axis=1, keepdims=True)
                    m_sc[h] = m_new
                    pv = jnp.dot(pm.astype(BF16), v2, preferred_element_type=F32)
                    acc2 = jnp.where(msk, alpha * acc2 + pv, acc2)
                acc_sc[:, lanes] = acc2

        @pl.when(j == i)
        def _():
            lane = lax.broadcasted_iota(jnp.int32, (1, LANES), 1)
            left = lane < HEAD
            lse = jnp.zeros((t, LANES), F32)
            for p in range(N_PAIRS):
                lanes = slice(p * LANES, (p + 1) * LANES)
                inv = jnp.where(left, 1.0 / l_sc[2 * p], 1.0 / l_sc[2 * p + 1])
                o = acc_sc[:, lanes] * inv
                o_ref[:, lanes] = o
                gate = g_ref[:, lanes]
                mix_ref[:, lanes] = o * (gate * _sigmoid(gate))
                for e in range(2):
                    h = 2 * p + e
                    lse = jnp.where(lane == h, m_sc[h] + jnp.log(l_sc[h]), lse)
            lse_ref[...] = lse

    qblk = pl.BlockSpec((t, D_HALF), lambda i, j: (i, 0))
    kblk = pl.BlockSpec((t, D_HALF), lambda i, j: (jnp.minimum(i, j), 0))
    return pl.pallas_call(
        body, name="fox_attn_fwd", grid=(nblk, nblk),
        in_specs=[qblk, kblk, kblk, pl.BlockSpec((t, LANES), lambda i, j: (i, 0)),
                  pl.BlockSpec((N_HEADS, t), lambda i, j: (0, jnp.minimum(i, j))),
                  pl.BlockSpec((t, D_HALF), lambda i, j: (i, 3))],
        out_specs=[qblk, qblk, pl.BlockSpec((t, LANES), lambda i, j: (i, 0))],
        out_shape=[jax.ShapeDtypeStruct((s, D_HALF), F32), jax.ShapeDtypeStruct((s, D_HALF), F32),
                   jax.ShapeDtypeStruct((s, LANES), F32)],
        scratch_shapes=[pltpu.VMEM((N_HEADS, t, 1), F32), pltpu.VMEM((N_HEADS, t, 1), F32),
                        pltpu.VMEM((t, D_HALF), F32)],
        compiler_params=_params("parallel", "arbitrary"),
    )(q, k, v, cc, cr, u_b)


def _fox_post_bwd(dmix, o, u_b, tm=256):
    s = o.shape[0]

    def body(dm_ref, o_ref, g_ref, do_ref, dg_ref):
        gate = g_ref[...]
        sg = _sigmoid(gate)
        dm = dm_ref[...]
        do_ref[...] = (dm * (gate * sg)).astype(BF16)
        dg_ref[...] = dm * o_ref[...] * (sg * (1.0 + gate * (1.0 - sg)))

    blk = pl.BlockSpec((tm, D_HALF), lambda i: (i, 0))
    return pl.pallas_call(
        body, name="fox_post_bwd", grid=(s // tm,),
        in_specs=[blk, blk, pl.BlockSpec((tm, D_HALF), lambda i: (i, 3))], out_specs=[blk] * 2,
        out_shape=[jax.ShapeDtypeStruct((s, D_HALF), BF16), jax.ShapeDtypeStruct((s, D_HALF), F32)],
        compiler_params=_params("parallel"),
    )(dmix, o, u_b)


def _attn_probs(q2, k2, v2, do2, msk, causal, bias, lse_col):
    zero = jnp.zeros_like(q2)
    qh = jnp.where(msk, q2, zero)
    doh = jnp.where(msk, do2, zero)
    sc = jnp.where(causal, _dot_nt(qh, k2) + bias, NEG)
    pm = jnp.exp(sc - lse_col)
    dp = _dot_nt(doh, v2)
    return qh, doh, pm, dp


def _attn_bwd_rowdot(q, k, v, do, lse, cc, cr):
    s = q.shape[0]
    t = ATT_T
    nblk = s // t

    def body(q_ref, k_ref, v_ref, do_ref, lse_ref, cc_ref, cr_ref, dd_ref, acc):
        i = pl.program_id(0)
        j = pl.program_id(1)

        @pl.when(j == 0)
        def _():
            acc[...] = jnp.zeros_like(acc)

        @pl.when(j <= i)
        def _():
            row = i * t + lax.broadcasted_iota(jnp.int32, (t, t), 0)
            col = j * t + lax.broadcasted_iota(jnp.int32, (t, t), 1)
            causal = row >= col
            left = lax.broadcasted_iota(jnp.int32, (1, LANES), 1) < HEAD
            for p in range(N_PAIRS):
                lanes = slice(p * LANES, (p + 1) * LANES)
                q2, k2, v2, do2 = q_ref[:, lanes], k_ref[:, lanes], v_ref[:, lanes], do_ref[:, lanes]
                for e in range(2):
                    h = 2 * p + e
                    msk = left if e == 0 else jnp.logical_not(left)
                    bias = cc_ref[:, h:h + 1] - cr_ref[h:h + 1, :]
                    _, _, pm, dp = _attn_probs(q2, k2, v2, do2, msk, causal, bias, lse_ref[:, h:h + 1])
                    acc[h] += jnp.sum(pm * dp, axis=1, keepdims=True)

        @pl.when(j == i)
        def _():
            lane = lax.broadcasted_iota(jnp.int32, (1, LANES), 1)
            dd = jnp.zeros((t, LANES), F32)
            for h in range(N_HEADS):
                dd = jnp.where(lane == h, acc[h], dd)
            dd_ref[...] = dd

    qblk = pl.BlockSpec((t, D_HALF), lambda i, j: (i, 0))
    qcol = pl.BlockSpec((t, LANES), lambda i, j: (i, 0))
    kblk = pl.BlockSpec((t, D_HALF), lambda i, j: (jnp.minimum(i, j), 0))
    return pl.pallas_call(
        body, name="fox_attn_rowdot", grid=(nblk, nblk),
        in_specs=[qblk, kblk, kblk, qblk, qcol, qcol, pl.BlockSpec((N_HEADS, t), lambda i, j: (0, jnp.minimum(i, j)))],
        out_specs=qcol, out_shape=jax.ShapeDtypeStruct((s, LANES), F32),
        scratch_shapes=[pltpu.VMEM((N_HEADS, t, 1), F32)],
        compiler_params=_params("parallel", "arbitrary"),
    )(q, k, v, do, lse, cc, cr)


def _attn_bwd(q, k, v, do, lse, dd, cc, cr):
    s = q.shape[0]
    t = ATT_T
    nblk = s // t

    def body(q_ref, k_ref, v_ref, do_ref, lse_ref, dd_ref, cc_ref, cr_ref,
             dq_ref, dk_ref, dv_ref, dcr_ref, dk_sc, dv_sc, dcr_sc):
        j = pl.program_id(0)
        i = pl.program_id(1)

        @pl.when(jnp.logical_and(j == 0, i == 0))
        def _():
            dq_ref[...] = jnp.zeros_like(dq_ref)

        @pl.when(i == 0)
        def _():
            dk_sc[...] = jnp.zeros_like(dk_sc)
            dv_sc[...] = jnp.zeros_like(dv_sc)
            dcr_sc[...] = jnp.zeros_like(dcr_sc)

        @pl.when(i >= j)
        def _():
            row = i * t + lax.broadcasted_iota(jnp.int32, (t, t), 0)
            col = j * t + lax.broadcasted_iota(jnp.int32, (t, t), 1)
            causal = row >= col
            left = lax.broadcasted_iota(jnp.int32, (1, LANES), 1) < HEAD
            qrows = pl.ds(pl.multiple_of(i * t, t), t)
            for p in range(N_PAIRS):
                lanes = slice(p * LANES, (p + 1) * LANES)
                q2, k2, v2, do2 = q_ref[:, lanes], k_ref[:, lanes], v_ref[:, lanes], do_ref[:, lanes]
                zero = jnp.zeros_like(q2)
                dq2 = jnp.zeros((t, LANES), F32)
                dk2 = jnp.zeros((t, LANES), F32)
                dv2 = jnp.zeros((t, LANES), F32)
                for e in range(2):
                    h = 2 * p + e
                    msk = left if e == 0 else jnp.logical_not(left)
                    bias = cc_ref[:, h:h + 1] - cr_ref[h:h + 1, :]
                    qh, doh, pm, dp = _attn_probs(q2, k2, v2, do2, msk, causal, bias, lse_ref[:, h:h + 1])
                    dsc = pm * (dp - dd_ref[:, h:h + 1])
                    dsb = dsc.astype(BF16)
                    dv2 += _dot_tn(pm.astype(BF16), doh)
                    dk2 += _dot_tn(dsb, qh)
                    dq2 += jnp.dot(dsb, jnp.where(msk, k2, zero), preferred_element_type=F32)
                    dcr_sc[h:h + 1, :] += -_colsum(dsc)
                dq_ref[qrows, lanes] += dq2 * ATT_SCALE
                dk_sc[:, lanes] += dk2
                dv_sc[:, lanes] += dv2

        @pl.when(i == nblk - 1)
        def _():
            dk_ref[...] = dk_sc[...]
            dv_ref[...] = dv_sc[...]
            dcr_ref[...] = dcr_sc[...]

    qblk = pl.BlockSpec((t, D_HALF), lambda j, i: (jnp.maximum(i, j), 0))
    qcol = pl.BlockSpec((t, LANES), lambda j, i: (jnp.maximum(i, j), 0))
    kblk = pl.BlockSpec((t, D_HALF), lambda j, i: (j, 0))
    return pl.pallas_call(
        body, name="fox_attn_bwd", grid=(nblk, nblk),
        in_specs=[qblk, kblk, kblk, qblk, qcol, qcol, qcol, pl.BlockSpec((N_HEADS, t), lambda j, i: (0, j))],
        out_specs=[pl.BlockSpec((s, D_HALF), lambda j, i: (0, 0)), kblk, kblk,
                   pl.BlockSpec((N_HEADS, t), lambda j, i: (0, j))],
        out_shape=[jax.ShapeDtypeStruct((s, D_HALF), F32)] * 3 + [jax.ShapeDtypeStruct((N_HEADS, s), F32)],
        scratch_shapes=[pltpu.VMEM((t, D_HALF), F32), pltpu.VMEM((t, D_HALF), F32), pltpu.VMEM((N_HEADS, t), F32)],
        compiler_params=_params("arbitrary", "arbitrary"),
    )(q, k, v, do, lse, dd, cc, cr)


def _fox_prep_bwd(u_b, dq, dk, dv, dgate, dcum, fb, qg, kg, tm=256):
    s = u_b.shape[0]
    nb = s // tm

    def body(ub_ref, dq_ref, dk_ref, dv_ref, dg_ref, dc_ref, fb_ref, qg_ref, kg_ref,
             du_ref, dqg_ref, dkg_ref, dfb_ref, carry):
        i = pl.program_id(0)

        @pl.when(i == 0)
        def _():
            carry[...] = jnp.zeros_like(carry)
            dqg_ref[...] = jnp.zeros_like(dqg_ref)
            dkg_ref[...] = jnp.zeros_like(dkg_ref)
            dfb_ref[...] = jnp.zeros_like(dfb_ref)

        bd = _head_ones()
        for lo, g_ref, d_ref, dgain_ref in ((0, qg_ref, dq_ref, dqg_ref), (512, kg_ref, dk_ref, dkg_ref)):
            gain = g_ref[...]
            xh, rinv, _ = _head_rms(ub_ref[:, lo:lo + 512], gain, bd)
            dn = d_ref[...]
            dgain_ref[...] += _colsum(dn * xh)
            dxh = dn * gain
            du_ref[:, lo:lo + 512] = rinv * (dxh - xh * (_head_sum(dxh * xh, bd) * (1.0 / HEAD)))
        du_ref[:, 1024:1536] = dv_ref[...]
        du_ref[:, 1536:2048] = dg_ref[...]
        lane = lax.broadcasted_iota(jnp.int32, (1, LANES), 1)
        dc = dc_ref[...]
        dlogf = _exact_dot(dc, _tri(tm, False), ones_first=True) + carry[...]
        carry[...] += _colsum(dc)
        fl = ub_ref[:, 2048:2176] + fb_ref[...]
        dfl = jnp.where(lane < N_HEADS, dlogf * (1.0 - _sigmoid(fl)), 0.0)
        du_ref[:, 2048:2176] = dfl
        dfb_ref[...] += _colsum(dfl)

    rev = lambda w: pl.BlockSpec((tm, w), lambda i: (nb - 1 - i, 0))
    vec = lambda w: pl.BlockSpec((1, w), lambda i: (0, 0))
    return pl.pallas_call(
        body, name="fox_prep_bwd", grid=(nb,),
        in_specs=[rev(SEC)] + [rev(D_HALF)] * 4 + [rev(LANES), vec(LANES), vec(D_HALF), vec(D_HALF)],
        out_specs=[rev(SEC), vec(D_HALF), vec(D_HALF), vec(LANES)],
        out_shape=[jax.ShapeDtypeStruct((s, SEC), F32), jax.ShapeDtypeStruct((1, D_HALF), F32),
                   jax.ShapeDtypeStruct((1, D_HALF), F32), jax.ShapeDtypeStruct((1, LANES), F32)],
        scratch_shapes=[pltpu.VMEM((1, LANES), F32)],
        compiler_params=_params("arbitrary"),
    )(u_b, dq, dk, dv, dgate, dcum, fb, qg, kg)


def _merge(mix_a, mix_b, u_g, x, tgt, wa, wb, wo, fg, tm=256):
    s, d = x.shape

    def body(ma_ref, mb_ref, ug_ref, x_ref, t_ref, wa_ref, wb_ref, wo_ref, fg_ref,
             dx2_ref, dma_ref, dmb_ref, dug_ref, dwa_ref, dwb_ref, dwo_ref, dfg_ref, loss_ref):
        i = pl.program_id(0)

        @pl.when(i == 0)
        def _():
            for ref in (dwa_ref, dwb_ref, dwo_ref, dfg_ref, loss_ref):
                ref[...] = jnp.zeros_like(ref)

        wa_v, wb_v, wo_v, fg_v = wa_ref[...], wb_ref[...], wo_ref[...], fg_ref[...]
        ma = ma_ref[...].astype(BF16)
        mb = mb_ref[...].astype(BF16)
        ya = jnp.dot(ma, wa_v, preferred_element_type=F32)
        yb = jnp.dot(mb, wb_v, preferred_element_type=F32)
        sa = _sigmoid(ug_ref[:, 0:d])
        sb = _sigmoid(ug_ref[:, d:2 * d])
        merged = (sa * ya + sb * yb).astype(BF16)
        x2 = x_ref[...] + jnp.dot(merged, wo_v, preferred_element_type=F32)
        r2 = lax.rsqrt(jnp.mean(x2 * x2, axis=-1, keepdims=True) + RMS_EPS)
        x2h = x2 * r2
        err = x2h * fg_v - t_ref[...]
        loss_ref[...] += _colsum(err * err)
        dy = err * (1.0 / d)
        dfg_ref[...] += _colsum(dy * x2h)
        dx2h = dy * fg_v
        dx2 = r2 * (dx2h - x2h * jnp.mean(dx2h * x2h, axis=-1, keepdims=True))
        dx2_ref[...] = dx2
        dx2b = dx2.astype(BF16)
        dmerged = _dot_nt(dx2b, wo_v)
        dwo_ref[...] += _dot_tn(merged, dx2b)
        dya = dmerged * sa
        dyb = dmerged * sb
        dug_ref[:, 0:d] = dya * ya * (1.0 - sa)
        dug_ref[:, d:2 * d] = dyb * yb * (1.0 - sb)
        dyab = dya.astype(BF16)
        dybb = dyb.astype(BF16)
        dma_ref[...] = _dot_nt(dyab, wa_v)
        dmb_ref[...] = _dot_nt(dybb, wb_v)
        dwa_ref[...] += _dot_tn(ma, dyab)
        dwb_ref[...] += _dot_tn(mb, dybb)

    row = lambda w: pl.BlockSpec((tm, w), lambda i: (i, 0))
    full = lambda a: pl.BlockSpec(a.shape, lambda i: (0, 0))
    fshape = lambda a: jax.ShapeDtypeStruct(a.shape, F32)
    return pl.pallas_call(
        body, name="merge_fwd_bwd", grid=(s // tm,),
        in_specs=[row(D_HALF), row(D_HALF), row(GATE_COLS), row(d), row(d), full(wa), full(wb), full(wo), full(fg)],
        out_specs=[row(d), row(D_HALF), row(D_HALF), row(GATE_COLS), full(wa), full(wb), full(wo), full(fg), full(fg)],
        out_shape=[jax.ShapeDtypeStruct((s, d), F32), jax.ShapeDtypeStruct((s, D_HALF), F32),
                   jax.ShapeDtypeStruct((s, D_HALF), F32), jax.ShapeDtypeStruct((s, GATE_COLS), F32),
                   fshape(wa), fshape(wb), fshape(wo), fshape(fg), fshape(fg)],
        compiler_params=_params("arbitrary"),
    )(mix_a, mix_b, u_g, x, tgt, wa, wb, wo, fg)


def _lora_weight(w_up, a_up):
    z = jnp.zeros((LORA, D_HALF), w_up.dtype)
    return jnp.concatenate([jnp.concatenate([w_up, z], axis=1), jnp.concatenate([z, a_up], axis=1)], axis=0)


def _device_grads(x, tgt, norm_g, w_a, w_b, w_g, shift_mu, w_up, w0, a_up, a0, k_k, k_a, r_k, lnx_w, lnx_b,
                  f_bias, q_norm_g, k_norm_g, w_out_a, w_out_b, w_out, final_norm_g):
    wl = _lora_weight(w_up, a_up)
    rk = r_k.reshape(1, D_HALF)
    fb = jnp.pad(f_bias, ((0, 0), (0, LANES - N_HEADS)))
    qg = jnp.tile(q_norm_g, (1, N_HEADS))
    kg = jnp.tile(k_norm_g, (1, N_HEADS))
    fg = final_norm_g.reshape(1, D_MODEL)

    h = _rmsnorm_in(x, norm_g)
    u_a = _matmul_nn(h, w_a, "inproj_rwkv")
    u_b = _matmul_nn(h, w_b, "inproj_fox")
    u_g = _matmul_nn(h, w_g, "inproj_gate")

    r, dec, k, v, av, bv, gate_a = _rwkv_prep(u_a, shift_mu, wl, w0, a0, k_k, k_a)
    y, st = _wkv_fwd(r, dec, k, av, bv, v)
    mix_a = _rwkv_post(y, r, k, v, gate_a, lnx_w, lnx_b, rk)

    q, kn, vb, cc, cr = _fox_prep(u_b, fb, qg, kg)
    o, mix_b, lse = _attn_fwd(q, kn, vb, cc, cr, u_b)

    dx2, dmix_a, dmix_b, du_g, dwa, dwb, dwo, dfg, loss_vec = _merge(
        mix_a, mix_b, u_g, x, tgt, w_out_a, w_out_b, w_out, fg)

    do, dgate_b = _fox_post_bwd(dmix_b, o, u_b)
    dd = _attn_bwd_rowdot(q, kn, vb, do, lse, cc, cr)
    dq, dk_att, dv_att, dcr = _attn_bwd(q, kn, vb, do, lse, dd, cc, cr)
    dcum = jnp.pad(dcr.T, ((0, 0), (0, LANES - N_HEADS)))
    du_b, dqg, dkg, dfb = _fox_prep_bwd(u_b, dq, dk_att, dv_att, dgate_b, dcum, fb, qg, kg)

    dy, dr_b, dk_b, dv_b, dgate_a, dlw, dlb, drk = _rwkv_post_bwd(dmix_a, y, r, k, v, gate_a, lnx_w, lnx_b, rk)
    dr_s, dw_s, dk_s, dv_s, da_s, db_s = _wkv_bwd(r, dec, k, av, bv, v, dy, st)
    du_a, dmu, dwl, dw0, da0, dkkw, dkaw = _rwkv_prep_bwd(
        u_a, (dr_s, dw_s, dk_s, dv_s, da_s, db_s, dr_b, dk_b, dv_b, dgate_a), shift_mu, wl, w0, a0, k_k, k_a)

    h_t = h.T
    dw_a = _matmul_tn_acc(h_t, du_a, "dw_rwkv")
    dw_b = _matmul_tn_acc(h_t, du_b, "dw_fox")
    dw_g = _matmul_tn_acc(h_t, du_g, "dw_gate")
    grad_x, dnorm_g = _inproj_bwd(du_a, du_b, du_g, w_a, w_b, w_g, x, dx2, norm_g)

    grads = dict(
        norm_g=dnorm_g, w_in=jnp.concatenate([dw_a, dw_b[:, :FOX_REAL], dw_g], axis=1), shift_mu=dmu,
        w_lora_up=dwl[:LORA, :D_HALF], w0=dw0, a_lora_up=dwl[LORA:, D_HALF:], a0=da0, k_k=dkkw, k_a=dkaw,
        r_k=drk.reshape(1, N_HEADS, HEAD), lnx_w=dlw, lnx_b=dlb, f_bias=dfb[:, :N_HEADS],
        q_norm_g=dqg.reshape(N_HEADS, HEAD).sum(axis=0, keepdims=True),
        k_norm_g=dkg.reshape(N_HEADS, HEAD).sum(axis=0, keepdims=True),
        w_out_a=dwa, w_out_b=dwb, w_out=dwo, final_norm_g=dfg.reshape(D_MODEL))
    return loss_vec, grad_x, grads


CHIP_FLIPS = ((1, 0), (0, 1), (1, 1))
ANY = pl.BlockSpec(memory_space=pl.ANY)


def _position():
    return lax.axis_index("x"), lax.axis_index("y"), lax.axis_index("c")


def _flip(v, f):
    return 1 - v if f else v


def _gather_shards(shards):
    n = len(shards)
    n_remote = len(CHIP_FLIPS) * n

    def body(*refs):
        ins, outs = refs[:n], refs[n:2 * n]
        send_sems, recv_sems, local_sems = refs[2 * n:]
        x, y, c = _position()
        me = 2 * x + y
        local = [pltpu.make_async_copy(ins[t], outs[t].at[me], local_sems.at[t]) for t in range(n)]
        for cp in local:
            cp.start()
        sends, recvs = [], []
        for f, (fx, fy) in enumerate(CHIP_FLIPS):
            px, py = _flip(x, fx), _flip(y, fy)
            for t in range(n):
                sems = dict(send_sem=send_sems.at[f * n + t], recv_sem=recv_sems.at[f * n + t],
                            device_id=(px, py, c), device_id_type=MESH)
                sends.append(pltpu.make_async_remote_copy(src_ref=ins[t], dst_ref=outs[t].at[me], **sems))
                recvs.append(pltpu.make_async_remote_copy(src_ref=ins[t], dst_ref=outs[t].at[2 * px + py], **sems))
        for cp in sends:
            cp.start()
        for cp in recvs:
            cp.wait_recv()
        for cp in sends:
            cp.wait_send()
        for cp in local:
            cp.wait()

    return pl.pallas_call(
        body, name="gather_weights", in_specs=[ANY] * n, out_specs=[ANY] * n,
        out_shape=[jax.ShapeDtypeStruct((N_CHIPS,) + a.shape, a.dtype) for a in shards],
        scratch_shapes=[pltpu.SemaphoreType.DMA((n_remote,)), pltpu.SemaphoreType.DMA((n_remote,)),
                        pltpu.SemaphoreType.DMA((n,))],
        compiler_params=pltpu.CompilerParams(has_side_effects=True),
    )(*shards)


def _scatter_partials(stacks):
    n = len(stacks)
    n_remote = len(CHIP_FLIPS) * n

    def body(*refs):
        ins, outs = refs[:n], refs[n:2 * n]
        send_sems, recv_sems = refs[2 * n:]
        x, y, c = _position()
        sends, recvs = [], []
        for f, (fx, fy) in enumerate(CHIP_FLIPS):
            px, py = _flip(x, fx), _flip(y, fy)
            for t in range(n):
                cp = pltpu.make_async_remote_copy(
                    src_ref=ins[t].at[2 * px + py], dst_ref=outs[t].at[f],
                    send_sem=send_sems.at[f * n + t], recv_sem=recv_sems.at[f * n + t],
                    device_id=(px, py, c), device_id_type=MESH)
                sends.append(cp)
        for cp in sends:
            cp.start()
        for cp in sends:
            cp.wait_recv()
        for cp in sends:
            cp.wait_send()

    return pl.pallas_call(
        body, name="scatter_partials", in_specs=[ANY] * n, out_specs=[ANY] * n,
        out_shape=[jax.ShapeDtypeStruct((len(CHIP_FLIPS),) + a.shape[1:], a.dtype) for a in stacks],
        scratch_shapes=[pltpu.SemaphoreType.DMA((n_remote,)), pltpu.SemaphoreType.DMA((n_remote,))],
        compiler_params=pltpu.CompilerParams(has_side_effects=True),
    )(*stacks)


def _swap_sibling(tensors):
    n = len(tensors)

    def body(*refs):
        ins, outs = refs[:n], refs[n:2 * n]
        send_sems, recv_sems = refs[2 * n:]
        x, y, c = _position()
        copies = [pltpu.make_async_remote_copy(
            src_ref=ins[t], dst_ref=outs[t], send_sem=send_sems.at[t], recv_sem=recv_sems.at[t],
            device_id=(x, y, 1 - c), device_id_type=MESH) for t in range(n)]
        for cp in copies:
            cp.start()
        for cp in copies:
            cp.wait_recv()
        for cp in copies:
            cp.wait_send()

    return pl.pallas_call(
        body, name="swap_sibling", in_specs=[ANY] * n, out_specs=[ANY] * n,
        out_shape=[jax.ShapeDtypeStruct(a.shape, a.dtype) for a in tensors],
        scratch_shapes=[pltpu.SemaphoreType.DMA((n,)), pltpu.SemaphoreType.DMA((n,))],
        compiler_params=pltpu.CompilerParams(has_side_effects=True),
    )(*tensors)


def _allreduce_small(slab):
    stages = 3

    def body(x_ref, o_ref, buf, send_sems, recv_sems):
        x, y, c = _position()
        peers = ((1 - x, y, c), (x, 1 - y, c), (x, y, 1 - c))
        o_ref[...] = x_ref[...]
        for k, peer in enumerate(peers):
            cp = pltpu.make_async_remote_copy(src_ref=o_ref, dst_ref=buf.at[k], send_sem=send_sems.at[k],
                                              recv_sem=recv_sems.at[k], device_id=peer, device_id_type=MESH)
            cp.start()
            cp.wait()
            o_ref[...] = o_ref[...] + buf[k]

    return pl.pallas_call(
        body, name="allreduce_small",
        in_specs=[pl.BlockSpec(memory_space=pltpu.VMEM)], out_specs=pl.BlockSpec(memory_space=pltpu.VMEM),
        out_shape=jax.ShapeDtypeStruct(slab.shape, slab.dtype),
        scratch_shapes=[pltpu.VMEM((stages,) + slab.shape, slab.dtype),
                        pltpu.SemaphoreType.DMA((stages,)), pltpu.SemaphoreType.DMA((stages,))],
        compiler_params=pltpu.CompilerParams(has_side_effects=True),
    )(slab)


def _row_tile(r):
    return min(r, 256)


def _sum4(stack, recv, me):
    _, r, c = stack.shape
    tr = _row_tile(r)

    def body(me_ref, own_ref, recv_ref, o_ref):
        o_ref[...] = (((own_ref[...] + recv_ref[0].astype(F32)) + recv_ref[1].astype(F32))
                      + recv_ref[2].astype(F32))

    return pl.pallas_call(
        body, name="sum_partials",
        grid_spec=pltpu.PrefetchScalarGridSpec(
            num_scalar_prefetch=1, grid=(r // tr,),
            in_specs=[pl.BlockSpec((None, tr, c), lambda i, me_ref: (me_ref[0], i, 0)),
                      pl.BlockSpec((len(CHIP_FLIPS), tr, c), lambda i, me_ref: (0, i, 0))],
            out_specs=pl.BlockSpec((tr, c), lambda i, me_ref: (i, 0))),
        out_shape=jax.ShapeDtypeStruct((r, c), F32), compiler_params=_params("parallel"),
    )(me, stack, recv)


def _adamw_math(w, g, m, v):
    m = ADAM_B1 * m + (1.0 - ADAM_B1) * g
    v = ADAM_B2 * v + (1.0 - ADAM_B2) * (g * g)
    m_hat = m / (1.0 - ADAM_B1 ** ADAM_STEP)
    v_hat = v / (1.0 - ADAM_B2 ** ADAM_STEP)
    delta = -ADAM_LR * (m_hat / (jnp.sqrt(v_hat) + ADAM_EPS) + ADAM_WD * w)
    return delta, m, v


def _adamw(w, m, v, g_parts, name):
    r, c = w.shape
    tr = _row_tile(r)
    n = len(g_parts)

    def body(*refs):
        w_ref, m_ref, v_ref = refs[:3]
        g_refs = refs[3:3 + n]
        g_out, d_out, m_out, v_out = refs[3 + n:]
        g = g_refs[0][...]
        for ref in g_refs[1:]:
            g = g + ref[...]
        g_out[...] = g
        d_out[...], m_out[...], v_out[...] = _adamw_math(w_ref[...], g, m_ref[...], v_ref[...])

    blk = pl.BlockSpec((tr, c), lambda i: (i, 0))
    return pl.pallas_call(
        body, name=name, grid=(r // tr,), in_specs=[blk] * (3 + n), out_specs=[blk] * 4,
        out_shape=[jax.ShapeDtypeStruct((r, c), F32)] * 4, compiler_params=_params("parallel"),
    )(w, m, v, *g_parts)


SHARDED = ("w_in", "w_lora_up", "a_lora_up", "w_out_a", "w_out_b", "w_out")
ROW_SHARDED = ("w_out",)
SMALL = ("norm_g", "shift_mu", "w0", "a0", "k_k", "k_a", "r_k", "lnx_w", "lnx_b", "f_bias", "q_norm_g", "k_norm_g",
         "final_norm_g")
WEIGHTS = ("norm_g", "w_in", "shift_mu", "w_lora_up", "w0", "a_lora_up", "a0", "k_k", "k_a", "r_k", "lnx_w", "lnx_b",
           "f_bias", "q_norm_g", "k_norm_g", "w_out_a", "w_out_b", "w_out", "final_norm_g")
SLAB_ROWS = 16
SLAB_COLS = SEC


def _to_slab(named, extra=None):
    rows = [jnp.pad(named[n].reshape(1, -1), ((0, 0), (0, SLAB_COLS - named[n].size))) for n in SMALL]
    if extra is not None:
        rows.append(jnp.pad(extra.reshape(1, -1), ((0, 0), (0, SLAB_COLS - extra.size))))
    rows.append(jnp.zeros((SLAB_ROWS - len(rows), SLAB_COLS), F32))
    return jnp.concatenate(rows, axis=0)


def _from_slab(slab, shapes):
    return {n: slab[i, :math.prod(shapes[n])].reshape(shapes[n]) for i, n in enumerate(SMALL)}


def _by_chip(g, name):
    if name in ROW_SHARDED:
        return g.reshape(N_CHIPS, g.shape[0] // N_CHIPS, g.shape[1])
    r, c = g.shape
    return g.reshape(r, N_CHIPS, c // N_CHIPS).transpose(1, 0, 2)


def _from_chips(stack, name):
    if name in ROW_SHARDED:
        return stack.reshape(-1, stack.shape[2])
    _, r, c = stack.shape
    return stack.transpose(1, 0, 2).reshape(r, N_CHIPS * c)


def kernel(x, norm_g, w_in, shift_mu, w_lora_up, w0, a_lora_up, a0, k_k, k_a, r_k, lnx_w, lnx_b, f_bias, q_norm_g, k_norm_g, w_out_a, w_out_b, w_out, final_norm_g, loss_target, m_norm_g, m_w_in, m_shift_mu, m_w_lora_up, m_w0, m_a_lora_up, m_a0, m_k_k, m_k_a, m_r_k, m_lnx_w, m_lnx_b, m_f_bias, m_q_norm_g, m_k_norm_g, m_w_out_a, m_w_out_b, m_w_out, m_final_norm_g, v_norm_g, v_w_in, v_shift_mu, v_w_lora_up, v_w0, v_a_lora_up, v_a0, v_k_k, v_k_a, v_r_k, v_lnx_w, v_lnx_b, v_f_bias, v_q_norm_g, v_k_norm_g, v_w_out_a, v_w_out_b, v_w_out, v_final_norm_g):
    w = dict(norm_g=norm_g, w_in=w_in, shift_mu=shift_mu, w_lora_up=w_lora_up, w0=w0, a_lora_up=a_lora_up, a0=a0,
             k_k=k_k, k_a=k_a, r_k=r_k, lnx_w=lnx_w, lnx_b=lnx_b, f_bias=f_bias, q_norm_g=q_norm_g,
             k_norm_g=k_norm_g, w_out_a=w_out_a, w_out_b=w_out_b, w_out=w_out, final_norm_g=final_norm_g)
    m = dict(norm_g=m_norm_g, w_in=m_w_in, shift_mu=m_shift_mu, w_lora_up=m_w_lora_up, w0=m_w0,
             a_lora_up=m_a_lora_up, a0=m_a0, k_k=m_k_k, k_a=m_k_a, r_k=m_r_k, lnx_w=m_lnx_w, lnx_b=m_lnx_b,
             f_bias=m_f_bias, q_norm_g=m_q_norm_g, k_norm_g=m_k_norm_g, w_out_a=m_w_out_a, w_out_b=m_w_out_b,
             w_out=m_w_out, final_norm_g=m_final_norm_g)
    v = dict(norm_g=v_norm_g, w_in=v_w_in, shift_mu=v_shift_mu, w_lora_up=v_w_lora_up, w0=v_w0,
             a_lora_up=v_a_lora_up, a0=v_a0, k_k=v_k_k, k_a=v_k_a, r_k=v_r_k, lnx_w=v_lnx_w, lnx_b=v_lnx_b,
             f_bias=v_f_bias, q_norm_g=v_q_norm_g, k_norm_g=v_k_norm_g, w_out_a=v_w_out_a, w_out_b=v_w_out_b,
             w_out=v_w_out, final_norm_g=v_final_norm_g)
    shapes = {n: w[n].shape for n in WEIGHTS}

    gathered = _gather_shards([w[n][0].astype(BF16) for n in SHARDED])
    full = {n: _from_chips(g, n) for n, g in zip(SHARDED, gathered)}
    w_full = full["w_in"]
    w_a = w_full[:, :RWKV_COLS]
    w_b = jnp.pad(w_full[:, RWKV_COLS:RWKV_COLS + FOX_REAL], ((0, 0), (0, SEC - FOX_REAL)))
    w_g = w_full[:, RWKV_COLS + FOX_REAL:]

    loss_vec, grad_x, grads = _device_grads(
        x[0], loss_target[0], norm_g, w_a, w_b, w_g, shift_mu, full["w_lora_up"], w0, full["a_lora_up"], a0, k_k, k_a,
        r_k, lnx_w, lnx_b, f_bias, q_norm_g, k_norm_g, full["w_out_a"], full["w_out_b"], full["w_out"], final_norm_g)

    total = _allreduce_small(_to_slab(grads, extra=loss_vec))
    loss = (0.5 / D_MODEL) * jnp.sum(total[len(SMALL)])
    slab_g, slab_d, slab_m, slab_v = _adamw(_to_slab(w), _to_slab(m), _to_slab(v), [total], "adamw_small")
    out_g, out_d, out_m, out_v = (_from_slab(s, shapes) for s in (total, slab_d, slab_m, slab_v))
    del slab_g

    xpos, ypos, _ = _position()
    me = (2 * xpos + ypos).astype(jnp.int32).reshape(1)
    stacks = [_by_chip(grads[n], n) for n in SHARDED]
    received = _scatter_partials([s.astype(BF16) for s in stacks])
    core_sums = [_sum4(s, r, me) for s, r in zip(stacks, received)]
    sibling_sums = _swap_sibling(core_sums)
    for n, mine, theirs in zip(SHARDED, core_sums, sibling_sums):
        g, d, m2, v2 = _adamw(w[n][0], m[n][0], v[n][0], [mine, theirs], "adamw_" + n)
        out_g[n], out_d[n], out_m[n], out_v[n] = (a.reshape(shapes[n]) for a in (g, d, m2, v2))

    return (loss, grad_x.reshape(x.shape), *[out_g[n] for n in WEIGHTS], *[out_d[n] for n in WEIGHTS],
            *[out_m[n] for n in WEIGHTS], *[out_v[n] for n in WEIGHTS])
```

```python
import functools
import math

import jax
import jax.numpy as jnp
from jax import lax
from jax.experimental import pallas as pl
from jax.experimental.pallas import tpu as pltpu

F32 = jnp.float32
BF16 = jnp.bfloat16

D_MODEL = 1024
D_HALF = 512
HEAD = 64
N_HEADS = 8
LORA = 64
RWKV_COLS = 2176
FOX_REAL = 2056
SEC = 2176
GATE_COLS = 2048
IN_COLS = 6280
N_CHIPS = 4
SHARD_COLS = IN_COLS // N_CHIPS
RMS_EPS = 1e-6
LNX_EPS = 64e-5
ATT_SCALE = HEAD ** -0.5
NEG = -1e30

ADAM_LR = 0.001
ADAM_B1 = 0.9
ADAM_B2 = 0.999
ADAM_EPS = 1e-08
ADAM_WD = 0.01
ADAM_STEP = 10

LANES = 128
SUBLANES = 8
VMEM_LIMIT = 56 * 1024 * 1024
MESH = pl.DeviceIdType.MESH


def _params(*sem):
    return pltpu.CompilerParams(dimension_semantics=sem if sem else None, vmem_limit_bytes=VMEM_LIMIT)


def _sigmoid(x):
    return 1.0 / (1.0 + jnp.exp(-x))


def _log_sigmoid(x):
    return jnp.minimum(x, 0.0) - jnp.log(1.0 + jnp.exp(-jnp.abs(x)))


def _head_ones():
    r = lax.broadcasted_iota(jnp.int32, (LANES, LANES), 0) >> 6
    c = lax.broadcasted_iota(jnp.int32, (LANES, LANES), 1) >> 6
    return (r == c).astype(BF16)


def _split3(x):
    hi = x.astype(BF16)
    r1 = x - hi.astype(F32)
    mid = r1.astype(BF16)
    lo = (r1 - mid.astype(F32)).astype(BF16)
    return hi, mid, lo


def _exact_dot(x, ones_bf16, ones_first=False):
    out = None
    for piece in _split3(x):
        if ones_first:
            t = jnp.dot(ones_bf16, piece, preferred_element_type=F32)
        else:
            t = jnp.dot(piece, ones_bf16, preferred_element_type=F32)
        out = t if out is None else out + t
    return out


def _head_sum(x, bd):
    n = x.shape[1] // LANES
    parts = [_exact_dot(x[:, i * LANES:(i + 1) * LANES], bd) for i in range(n)]
    return parts[0] if n == 1 else jnp.concatenate(parts, axis=1)


def _dot_nt(a, b):
    return lax.dot_general(a, b, (((1,), (1,)), ((), ())), preferred_element_type=F32)


def _dot_tn(a, b):
    return lax.dot_general(a, b, (((0,), (0,)), ((), ())), preferred_element_type=F32)


def _colsum(x):
    return jnp.sum(x, axis=0, keepdims=True)


def _rmsnorm_in(x, g, tm=512):
    s, d = x.shape

    def body(x_ref, g_ref, h_ref):
        xv = x_ref[...]
        r = lax.rsqrt(jnp.mean(xv * xv, axis=-1, keepdims=True) + RMS_EPS)
        h_ref[...] = (xv * r * g_ref[...]).astype(BF16)

    return pl.pallas_call(
        body, name="rmsnorm_in", grid=(s // tm,),
        in_specs=[pl.BlockSpec((tm, d), lambda i: (i, 0)), pl.BlockSpec((1, d), lambda i: (0, 0))],
        out_specs=pl.BlockSpec((tm, d), lambda i: (i, 0)),
        out_shape=jax.ShapeDtypeStruct((s, d), BF16), compiler_params=_params("parallel"),
    )(x, g)


def _matmul_nn(a, b, name, tm=512):
    m, k = a.shape
    n = b.shape[1]

    def body(a_ref, b_ref, o_ref):
        o_ref[...] = jnp.dot(a_ref[...], b_ref[...], preferred_element_type=F32)

    return pl.pallas_call(
        body, name=name, grid=(m // tm,),
        in_specs=[pl.BlockSpec((tm, k), lambda i: (i, 0)), pl.BlockSpec((k, n), lambda i: (0, 0))],
        out_specs=pl.BlockSpec((tm, n), lambda i: (i, 0)),
        out_shape=jax.ShapeDtypeStruct((m, n), F32), compiler_params=_params("parallel"),
    )(a, b)


def _matmul_tn_acc(at, b, name, tk=512):
    m, k = at.shape
    n = b.shape[1]

    def body(a_ref, b_ref, o_ref):
        j = pl.program_id(0)

        @pl.when(j == 0)
        def _():
            o_ref[...] = jnp.zeros_like(o_ref)

        o_ref[...] += jnp.dot(a_ref[...], b_ref[...].astype(BF16), preferred_element_type=F32)

    return pl.pallas_call(
        body, name=name, grid=(k // tk,),
        in_specs=[pl.BlockSpec((m, tk), lambda j: (0, j)), pl.BlockSpec((tk, n), lambda j: (j, 0))],
        out_specs=pl.BlockSpec((m, n), lambda j: (0, 0)),
        out_shape=jax.ShapeDtypeStruct((m, n), F32), compiler_params=_params("arbitrary"),
    )(at, b)


def _inproj_bwd(du_a, du_b, du_g, w_a, w_b, w_g, x, dx2, g, tm=256):
    s, d = x.shape

    def body(da_ref, db_ref, dg_ref, wa_ref, wb_ref, wg_ref, x_ref, dx2_ref, g_ref, gx_ref, gg_ref):
        i = pl.program_id(0)

        @pl.when(i == 0)
        def _():
            gg_ref[...] = jnp.zeros_like(gg_ref)

        dh = _dot_nt(da_ref[...].astype(BF16), wa_ref[...])
        dh += _dot_nt(db_ref[...].astype(BF16), wb_ref[...])
        dh += _dot_nt(dg_ref[...].astype(BF16), wg_ref[...])
        xv = x_ref[...]
        r = lax.rsqrt(jnp.mean(xv * xv, axis=-1, keepdims=True) + RMS_EPS)
        xh = xv * r
        gg_ref[...] += _colsum(dh * xh)
        dxh = dh * g_ref[...]
        gx_ref[...] = dx2_ref[...] + r * (dxh - xh * jnp.mean(dxh * xh, axis=-1, keepdims=True))

    row = lambda w: pl.BlockSpec((tm, w), lambda i: (i, 0))
    full = lambda a: pl.BlockSpec(a.shape, lambda i: (0, 0))
    return pl.pallas_call(
        body, name="inproj_bwd", grid=(s // tm,),
        in_specs=[row(SEC), row(SEC), row(GATE_COLS), full(w_a), full(w_b), full(w_g), row(d), row(d), full(g)],
        out_specs=[row(d), pl.BlockSpec((1, d), lambda i: (0, 0))],
        out_shape=[jax.ShapeDtypeStruct((s, d), F32), jax.ShapeDtypeStruct((1, d), F32)],
        compiler_params=_params("arbitrary"),
    )(du_a, du_b, du_g, w_a, w_b, w_g, x, dx2, g)


def _rwkv_elementwise(ua, prev_row, first, mu, wl, w0, a0, kkw, kaw, bd):
    tm = ua.shape[0]
    rows = lax.broadcasted_iota(jnp.int32, (tm, 1), 0)
    prev = jnp.where(first, jnp.zeros_like(prev_row), prev_row)
    shifted = jnp.where(rows == 0, prev, pltpu.roll(ua, 1, 0))
    delta = shifted - ua
    us = ua + delta * mu
    r = us[:, 0:512]
    k0 = us[:, 512:1024]
    v = us[:, 1024:1536]
    lo = us[:, 1536:1664]
    gate = us[:, 1664:2176]
    lane = lax.broadcasted_iota(jnp.int32, (1, LANES), 1)
    th = jnp.tanh(lo)
    lin = jnp.where(lane < LORA, th, lo)
    ll = jnp.dot(lin.astype(BF16), wl, preferred_element_type=F32)
    sz = _sigmoid(w0 + ll[:, :512])
    e = sz * math.exp(-0.5)
    dec = jnp.exp(-e)
    a = _sigmoid(a0 + ll[:, 512:])
    kk0 = k0 * kkw
    ss = _head_sum(kk0 * kk0, bd)
    nrm = jnp.maximum(jnp.sqrt(ss), 1e-12)
    kk = kk0 / nrm
    k = k0 * (1.0 + (a - 1.0) * kaw)
    return dict(delta=delta, us=us, r=r, k0=k0, v=v, lo=lo, gate=gate, th=th, lin=lin, sz=sz, e=e, dec=dec,
                a=a, kk0=kk0, ss=ss, nrm=nrm, kk=kk, k=k)


def _rwkv_prep(u_a, mu, wl, w0, a0, kkw, kaw, tm=256):
    s = u_a.shape[0]

    def body(ua_ref, prev_ref, mu_ref, wl_ref, w0_ref, a0_ref, kkw_ref, kaw_ref,
             r_ref, w_ref, k_ref, v_ref, a_ref, b_ref, g_ref):
        i = pl.program_id(0)
        f = _rwkv_elementwise(ua_ref[...], prev_ref[7:8, :], i == 0, mu_ref[...], wl_ref[...], w0_ref[...],
                              a0_ref[...], kkw_ref[...], kaw_ref[...], _head_ones())
        r_ref[...] = f["r"]
        w_ref[...] = f["dec"]
        k_ref[...] = f["k"]
        v_ref[...] = f["v"]
        a_ref[...] = -f["kk"]
        b_ref[...] = f["kk"] * f["a"]
        g_ref[...] = f["gate"]

    vec = lambda w: pl.BlockSpec((1, w), lambda i: (0, 0))
    out = pl.BlockSpec((tm, D_HALF), lambda i: (i, 0))
    return pl.pallas_call(
        body, name="rwkv_prep", grid=(s // tm,),
        in_specs=[pl.BlockSpec((tm, SEC), lambda i: (i, 0)),
                  pl.BlockSpec((8, SEC), lambda i: (jnp.maximum(i * (tm // 8) - 1, 0), 0)),
                  vec(SEC), pl.BlockSpec((LANES, 2 * D_HALF), lambda i: (0, 0)),
                  vec(D_HALF), vec(D_HALF), vec(D_HALF), vec(D_HALF)],
        out_specs=[out] * 7,
        out_shape=[jax.ShapeDtypeStruct((s, D_HALF), F32)] * 7,
        compiler_params=_params("parallel"),
    )(u_a, u_a, mu, wl, w0, a0, kkw, kaw)


SCAN_TB = 128
N_PAIRS = 4


def _pair_sum(x, left):
    s_l = jnp.sum(jnp.where(left, x, 0.0), axis=1, keepdims=True)
    s_r = jnp.sum(jnp.where(left, 0.0, x), axis=1, keepdims=True)
    return jnp.where(left, s_l, s_r)


def _quad_consts():
    lane = lax.broadcasted_iota(jnp.int32, (HEAD, 2 * LANES), 1)
    rowi = lax.broadcasted_iota(jnp.int32, (HEAD, 2 * LANES), 0)
    diag2 = rowi == (lane & (HEAD - 1))
    r = lax.broadcasted_iota(jnp.int32, (2 * LANES, 2 * LANES), 0) >> 6
    c = lax.broadcasted_iota(jnp.int32, (2 * LANES, 2 * LANES), 1) >> 6
    return diag2, (r == c).astype(BF16)


def _rows_to_columns(x8, diag2, bd2):
    hi = x8.astype(BF16).astype(F32)
    lo = x8 - hi
    lhs = jnp.concatenate([jnp.where(diag2, piece[i:i + 1], 0.0).astype(BF16)
                           for piece in (hi, lo) for i in range(SUBLANES)], axis=0)
    res = jnp.dot(lhs, bd2, preferred_element_type=F32)
    half = SUBLANES * HEAD
    return res[:half] + res[half:]


def _diag_rows(qtile, diag2, bd2, sub_row2):
    half = SUBLANES * HEAD
    res = jnp.dot(qtile, bd2, preferred_element_type=F32)
    out = jnp.zeros((SUBLANES, 2 * LANES), F32)
    for i in range(SUBLANES):
        t = res[i * HEAD:(i + 1) * HEAD] + res[half + i * HEAD:half + (i + 1) * HEAD]
        out = jnp.where(sub_row2 == i, _colsum(jnp.where(diag2, t, 0.0)), out)
    return out


def _store_pieces(qbuf, slot, p, i, x):
    half = SUBLANES * HEAD
    inner = slice((p % 2) * LANES, (p % 2 + 1) * LANES)
    hi = x.astype(BF16)
    qbuf[slot, p // 2, i * HEAD:(i + 1) * HEAD, inner] = hi
    qbuf[slot, p // 2, half + i * HEAD:half + (i + 1) * HEAD, inner] = (x - hi.astype(F32)).astype(BF16)


def _left_half():
    return lax.broadcasted_iota(jnp.int32, (HEAD, LANES), 1) < HEAD


def _wkv_fwd(r, w, k, a, b, v):
    s = r.shape[0]
    tb = SCAN_TB

    def body(r_ref, w_ref, k_ref, a_ref, b_ref, v_ref, y_ref, st_ref, state, vbuf, qbuf):
        g = pl.program_id(0)

        @pl.when(g == 0)
        def _():
            state[...] = jnp.zeros_like(state)
            qbuf[...] = jnp.zeros_like(qbuf)

        left = _left_half()
        diag2, bd2 = _quad_consts()
        sub_row2 = lax.broadcasted_iota(jnp.int32, (SUBLANES, 2 * LANES), 0)
        groups = tb // SUBLANES
        quads = [slice(g2 * 2 * LANES, (g2 + 1) * 2 * LANES) for g2 in range(2)]

        def rows_of(q):
            return pl.ds(pl.multiple_of(q * SUBLANES, SUBLANES), SUBLANES)

        def v_tiles(q, slot):
            v8 = v_ref[rows_of(q), :]
            for g2 in range(2):
                vbuf[slot, g2] = _rows_to_columns(v8[:, quads[g2]], diag2, bd2)

        def chain(q, slot):
            rows8 = rows_of(q)
            a8, w8, b8, k8, r8 = (x[rows8, :] for x in (a_ref, w_ref, b_ref, k_ref, r_ref))
            sp = [state[p] for p in range(N_PAIRS)]
            for i in range(SUBLANES):
                row = slice(i, i + 1)
                for p in range(N_PAIRS):
                    st_ref[q * SUBLANES + i, p] = sp[p]
                sa = [_pair_sum(sp[p] * a8[row, p * LANES:(p + 1) * LANES], left) for p in range(N_PAIRS)]
                for p in range(N_PAIRS):
                    lanes = slice(p * LANES, (p + 1) * LANES)
                    vt = vbuf[slot, p // 2, i * HEAD:(i + 1) * HEAD, (p % 2) * LANES:(p % 2 + 1) * LANES]
                    sp[p] = sp[p] * w8[row, lanes] + sa[p] * b8[row, lanes] + vt * k8[row, lanes]
                    _store_pieces(qbuf, slot, p, i, sp[p] * r8[row, lanes])
            for p in range(N_PAIRS):
                state[p] = sp[p]

        def y_rows(q, slot):
            for g2 in range(2):
                y_ref[rows_of(q), quads[g2]] = _diag_rows(qbuf[slot, g2], diag2, bd2, sub_row2)

        v_tiles(0, 0)

        def two_groups(j, carry):
            q0 = 2 * j
            v_tiles(q0 + 1, 1)
            chain(q0, 0)
            y_rows(jnp.maximum(q0 - 1, 0), 1)
            v_tiles(jnp.minimum(q0 + 2, groups - 1), 0)
            chain(q0 + 1, 1)
            y_rows(q0, 0)
            return carry

        lax.fori_loop(0, groups // 2, two_groups, 0)
        y_rows(groups - 1, 1)

    rows = pl.BlockSpec((tb, D_HALF), lambda g: (g, 0))
    return pl.pallas_call(
        body, name="wkv_fwd", grid=(s // tb,),
        in_specs=[rows] * 6,
        out_specs=[rows, pl.BlockSpec((tb, N_PAIRS, HEAD, LANES), lambda g: (g, 0, 0, 0))],
        out_shape=[jax.ShapeDtypeStruct((s, D_HALF), F32),
                   jax.ShapeDtypeStruct((s, N_PAIRS, HEAD, LANES), F32)],
        scratch_shapes=[pltpu.VMEM((N_PAIRS, HEAD, LANES), F32),
                        pltpu.VMEM((2, 2, SUBLANES * HEAD, 2 * LANES), F32),
                        pltpu.VMEM((2, 2, 2 * SUBLANES * HEAD, 2 * LANES), BF16)],
        compiler_params=_params("arbitrary"),
    )(r, w, k, a, b, v)


def _wkv_bwd(r, w, k, a, b, v, dy, st):
    s = r.shape[0]
    tb = SCAN_TB
    nb = s // tb

    def body(r_ref, w_ref, k_ref, a_ref, b_ref, v_ref, dy_ref, st_ref,
             dr_ref, dw_ref, dk_ref, dv_ref, da_ref, db_ref, dstate, vbuf, qbuf):
        g = pl.program_id(0)

        @pl.when(g == 0)
        def _():
            dstate[...] = jnp.zeros_like(dstate)
            qbuf[...] = jnp.zeros_like(qbuf)

        left = _left_half()
        diag2, bd2 = _quad_consts()
        sub_row = lax.broadcasted_iota(jnp.int32, (SUBLANES, LANES), 0)
        sub_row2 = lax.broadcasted_iota(jnp.int32, (SUBLANES, 2 * LANES), 0)
        groups = tb // SUBLANES
        quads = [slice(g2 * 2 * LANES, (g2 + 1) * 2 * LANES) for g2 in range(2)]
        row_refs = (dr_ref, dw_ref, dk_ref, da_ref, db_ref)

        def rows_of(q):
            return pl.ds(pl.multiple_of(q * SUBLANES, SUBLANES), SUBLANES)

        def column_tiles(q, slot):
            rows8 = rows_of(q)
            for kind, ref in enumerate((v_ref, dy_ref)):
                x8 = ref[rows8, :]
                for g2 in range(2):
                    vbuf[slot, kind, g2] = _rows_to_columns(x8[:, quads[g2]], diag2, bd2)

        def chain(q, slot):
            rows8 = rows_of(q)
            a8, w8, b8, k8, r8 = (x[rows8, :] for x in (a_ref, w_ref, b_ref, k_ref, r_ref))
            dsp = [dstate[p] for p in range(N_PAIRS)]
            outs = [[jnp.zeros((SUBLANES, LANES), F32) for _ in row_refs] for _ in range(N_PAIRS)]
            for i in reversed(range(SUBLANES)):
                row = slice(i, i + 1)
                pl_ = [slice(p * LANES, (p + 1) * LANES) for p in range(N_PAIRS)]
                tile = [(p // 2, slice(i * HEAD, (i + 1) * HEAD), slice((p % 2) * LANES, (p % 2 + 1) * LANES))
                        for p in range(N_PAIRS)]
                sp = [st_ref[q * SUBLANES + i, p] for p in range(N_PAIRS)]
                dyt = [vbuf[(slot, 1) + tile[p]] for p in range(N_PAIRS)]
                ds = [dsp[p] + dyt[p] * r8[row, pl_[p]] for p in range(N_PAIRS)]
                dsa = [_pair_sum(ds[p] * b8[row, pl_[p]], left) for p in range(N_PAIRS)]
                sa = [_pair_sum(sp[p] * a8[row, pl_[p]], left) for p in range(N_PAIRS)]
                for p in range(N_PAIRS):
                    ar, wr, br, kr = (x[row, pl_[p]] for x in (a8, w8, b8, k8))
                    vt = vbuf[(slot, 0) + tile[p]]
                    dsp[p] = ds[p] * wr + dsa[p] * ar
                    sn = sp[p] * wr + sa[p] * br + vt * kr
                    new = (_colsum(sn * dyt[p]), _colsum(ds[p] * sp[p]), _colsum(ds[p] * vt),
                           _colsum(sp[p] * dsa[p]), _colsum(ds[p] * sa[p]))
                    outs[p] = [jnp.where(sub_row == i, n, o) for n, o in zip(new, outs[p])]
                    _store_pieces(qbuf, slot, p, i, ds[p] * kr)
            for p in range(N_PAIRS):
                dstate[p] = dsp[p]
                for ref, o in zip(row_refs, outs[p]):
                    ref[rows8, p * LANES:(p + 1) * LANES] = o

        def dv_rows(q, slot):
            for g2 in range(2):
                dv_ref[rows_of(q), quads[g2]] = _diag_rows(qbuf[slot, g2], diag2, bd2, sub_row2)

        column_tiles(groups - 1, 0)

        def two_groups(j, carry):
            q0 = groups - 1 - 2 * j
            column_tiles(q0 - 1, 1)
            chain(q0, 0)
            dv_rows(jnp.minimum(q0 + 1, groups - 1), 1)
            column_tiles(jnp.maximum(q0 - 2, 0), 0)
            chain(q0 - 1, 1)
            dv_rows(q0, 0)
            return carry

        lax.fori_loop(0, groups // 2, two_groups, 0)
        dv_rows(0, 1)

    rows = pl.BlockSpec((tb, D_HALF), lambda g: (nb - 1 - g, 0))
    return pl.pallas_call(
        body, name="wkv_bwd", grid=(nb,),
        in_specs=[rows] * 7 + [pl.BlockSpec((tb, N_PAIRS, HEAD, LANES), lambda g: (nb - 1 - g, 0, 0, 0))],
        out_specs=[rows] * 6,
        out_shape=[jax.ShapeDtypeStruct((s, D_HALF), F32)] * 6,
        scratch_shapes=[pltpu.VMEM((N_PAIRS, HEAD, LANES), F32),
                        pltpu.VMEM((2, 2, 2, SUBLANES * HEAD, 2 * LANES), F32),
                        pltpu.VMEM((2, 2, 2 * SUBLANES * HEAD, 2 * LANES), BF16)],
        compiler_params=_params("arbitrary"),
    )(r, w, k, a, b, v, dy, st)


def _rwkv_post_math(y, r, k, v, gate, lw, lb, rk, bd):
    mean = _head_sum(y, bd) * (1.0 / HEAD)
    yc = y - mean
    var = _head_sum(yc * yc, bd) * (1.0 / HEAD)
    rstd = lax.rsqrt(var + LNX_EPS)
    yn = yc * rstd
    rkk = _head_sum(r * k * rk, bd)
    sg = _sigmoid(gate)
    pre = yn * lw + lb + rkk * v
    return yn, rstd, rkk, sg, pre


def _rwkv_post(y, r, k, v, gate, lw, lb, rk, tm=256):
    s = y.shape[0]

    def body(y_ref, r_ref, k_ref, v_ref, g_ref, lw_ref, lb_ref, rk_ref, o_ref):
        gate_v = g_ref[...]
        _, _, _, sg, pre = _rwkv_post_math(y_ref[...], r_ref[...], k_ref[...], v_ref[...], gate_v,
                                           lw_ref[...], lb_ref[...], rk_ref[...], _head_ones())
        o_ref[...] = pre * (gate_v * sg)

    blk = pl.BlockSpec((tm, D_HALF), lambda i: (i, 0))
    vec = pl.BlockSpec((1, D_HALF), lambda i: (0, 0))
    return pl.pallas_call(
        body, name="rwkv_post", grid=(s // tm,),
        in_specs=[blk] * 5 + [vec] * 3, out_specs=blk,
        out_shape=jax.ShapeDtypeStruct((s, D_HALF), F32), compiler_params=_params("parallel"),
    )(y, r, k, v, gate, lw, lb, rk)


def _rwkv_post_bwd(dmix, y, r, k, v, gate, lw, lb, rk, tm=256):
    s = y.shape[0]

    def body(dm_ref, y_ref, r_ref, k_ref, v_ref, g_ref, lw_ref, lb_ref, rk_ref,
             dy_ref, dr_ref, dk_ref, dv_ref, dg_ref, dlw_ref, dlb_ref, drk_ref):
        i = pl.program_id(0)

        @pl.when(i == 0)
        def _():
            dlw_ref[...] = jnp.zeros_like(dlw_ref)
            dlb_ref[...] = jnp.zeros_like(dlb_ref)
            drk_ref[...] = jnp.zeros_like(drk_ref)

        bd = _head_ones()
        rv, kv, vv, gate_v, lw_v, rk_v = r_ref[...], k_ref[...], v_ref[...], g_ref[...], lw_ref[...], rk_ref[...]
        yn, rstd, rkk, sg, pre = _rwkv_post_math(y_ref[...], rv, kv, vv, gate_v, lw_v, lb_ref[...], rk_v, bd)
        dm = dm_ref[...]
        dg_ref[...] = dm * pre * (sg * (1.0 + gate_v * (1.0 - sg)))
        dpre = dm * (gate_v * sg)
        dlw_ref[...] += _colsum(dpre * yn)
        dlb_ref[...] += _colsum(dpre)
        dyn = dpre * lw_v
        m1 = _head_sum(dyn, bd) * (1.0 / HEAD)
        m2 = _head_sum(dyn * yn, bd) * (1.0 / HEAD)
        dy_ref[...] = rstd * (dyn - m1 - yn * m2)
        dv_ref[...] = dpre * rkk
        drkk = _head_sum(dpre * vv, bd)
        dr_ref[...] = drkk * kv * rk_v
        dk_ref[...] = drkk * rv * rk_v
        drk_ref[...] += _colsum(drkk * rv * kv)

    blk = pl.BlockSpec((tm, D_HALF), lambda i: (i, 0))
    vec = pl.BlockSpec((1, D_HALF), lambda i: (0, 0))
    return pl.pallas_call(
        body, name="rwkv_post_bwd", grid=(s // tm,),
        in_specs=[blk] * 6 + [vec] * 3, out_specs=[blk] * 5 + [vec] * 3,
        out_shape=[jax.ShapeDtypeStruct((s, D_HALF), F32)] * 5 + [jax.ShapeDtypeStruct((1, D_HALF), F32)] * 3,
        compiler_params=_params("arbitrary"),
    )(dmix, y, r, k, v, gate, lw, lb, rk)


def _rwkv_prep_bwd(u_a, grads, mu, wl, w0, a0, kkw, kaw, tm=256):
    s = u_a.shape[0]
    nb = s // tm

    def body(ua_ref, prev_ref, drs_ref, dws_ref, dks_ref, dvs_ref, das_ref, dbs_ref, drb_ref, dkb_ref, dvb_ref,
             dgt_ref, mu_ref, wl_ref, w0_ref, a0_ref, kkw_ref, kaw_ref,
             du_ref, dmu_ref, dwl_ref, dw0_ref, da0_ref, dkkw_ref, dkaw_ref, carry):
        i = pl.program_id(0)

        @pl.when(i == 0)
        def _():
            carry[...] = jnp.zeros_like(carry)
            for ref in (dmu_ref, dwl_ref, dw0_ref, da0_ref, dkkw_ref, dkaw_ref):
                ref[...] = jnp.zeros_like(ref)

        bd = _head_ones()
        mu_v, wl_v, kkw_v, kaw_v = mu_ref[...], wl_ref[...], kkw_ref[...], kaw_ref[...]
        f = _rwkv_elementwise(ua_ref[...], prev_ref[7:8, :], i == nb - 1, mu_v, wl_v, w0_ref[...],
                              a0_ref[...], kkw_v, kaw_v, bd)
        a, kk, k0 = f["a"], f["kk"], f["k0"]
        dk = dks_ref[...] + dkb_ref[...]
        dbs = dbs_ref[...]
        dkk = dbs * a - das_ref[...]
        da = dbs * kk + dk * k0 * kaw_v
        dk0 = dk * (1.0 + (a - 1.0) * kaw_v)
        dkaw_ref[...] += _colsum(dk * k0 * (a - 1.0))
        inv = 1.0 / f["nrm"]
        proj = _head_sum(dkk * kk, bd)
        dkk0 = jnp.where(f["ss"] > 1e-24, (dkk - kk * proj) * inv, dkk * inv)
        dk0 = dk0 + dkk0 * kkw_v
        dkkw_ref[...] += _colsum(dkk0 * k0)
        dza = da * a * (1.0 - a)
        da0_ref[...] += _colsum(dza)
        dz = -dws_ref[...] * f["dec"] * f["e"] * (1.0 - f["sz"])
        dw0_ref[...] += _colsum(dz)
        dll = jnp.concatenate([dz, dza], axis=1).astype(BF16)
        dwl_ref[...] += _dot_tn(f["lin"].astype(BF16), dll)
        dlin = _dot_nt(dll, wl_v)
        lane = lax.broadcasted_iota(jnp.int32, (1, LANES), 1)
        th = f["th"]
        dlo = jnp.where(lane < LORA, dlin * (1.0 - th * th), dlin)
        dus = jnp.concatenate([drs_ref[...] + drb_ref[...], dk0, dvs_ref[...] + dvb_ref[...], dlo, dgt_ref[...]],
                              axis=1)
        dmu_ref[...] += _colsum(dus * f["delta"])
        g1 = dus * mu_v
        rows = lax.broadcasted_iota(jnp.int32, (tm, 1), 0)
        up = jnp.where(rows == tm - 1, carry[...], pltpu.roll(g1, tm - 1, 0))
        du_ref[...] = dus - g1 + up
        carry[...] = g1[0:1, :]

    rev = lambda w: pl.BlockSpec((tm, w), lambda i: (nb - 1 - i, 0))
    vec = lambda w: pl.BlockSpec((1, w), lambda i: (0, 0))
    wl_spec = pl.BlockSpec((LANES, 2 * D_HALF), lambda i: (0, 0))
    return pl.pallas_call(
        body, name="rwkv_prep_bwd", grid=(nb,),
        in_specs=[rev(SEC), pl.BlockSpec((8, SEC), lambda i: (jnp.maximum((nb - 1 - i) * (tm // 8) - 1, 0), 0))]
                 + [rev(D_HALF)] * 10 + [vec(SEC), wl_spec] + [vec(D_HALF)] * 4,
        out_specs=[rev(SEC), vec(SEC), wl_spec] + [vec(D_HALF)] * 4,
        out_shape=[jax.ShapeDtypeStruct((s, SEC), F32), jax.ShapeDtypeStruct((1, SEC), F32),
                   jax.ShapeDtypeStruct((LANES, 2 * D_HALF), F32)] + [jax.ShapeDtypeStruct((1, D_HALF), F32)] * 4,
        scratch_shapes=[pltpu.VMEM((1, SEC), F32)],
        compiler_params=_params("arbitrary"),
    )(u_a, u_a, *grads, mu, wl, w0, a0, kkw, kaw)


def _tri(tm, lower):
    r = lax.broadcasted_iota(jnp.int32, (tm, tm), 0)
    c = lax.broadcasted_iota(jnp.int32, (tm, tm), 1)
    return ((r >= c) if lower else (r <= c)).astype(BF16)


def _head_rms(x, g, bd):
    rinv = lax.rsqrt(_head_sum(x * x, bd) * (1.0 / HEAD) + RMS_EPS)
    xh = x * rinv
    return xh, rinv, xh * g


def _fox_prep(u_b, fb, qg, kg, tm=256):
    s = u_b.shape[0]

    def body(ub_ref, fb_ref, qg_ref, kg_ref, q_ref, k_ref, v_ref, cc_ref, cr_ref, carry):
        i = pl.program_id(0)

        @pl.when(i == 0)
        def _():
            carry[...] = jnp.zeros_like(carry)

        bd = _head_ones()
        _, _, qn = _head_rms(ub_ref[:, 0:512], qg_ref[...], bd)
        _, _, kn = _head_rms(ub_ref[:, 512:1024], kg_ref[...], bd)
        q_ref[...] = (qn * ATT_SCALE).astype(BF16)
        k_ref[...] = kn.astype(BF16)
        v_ref[...] = ub_ref[:, 1024:1536].astype(BF16)
        lane = lax.broadcasted_iota(jnp.int32, (1, LANES), 1)
        logf = jnp.where(lane < N_HEADS, _log_sigmoid(ub_ref[:, 2048:2176] + fb_ref[...]), 0.0)
        cum = _exact_dot(logf, _tri(tm, True), ones_first=True) + carry[...]
        cc_ref[...] = cum
        cr_ref[...] = jnp.transpose(cum)[0:N_HEADS, :]
        carry[...] = cum[tm - 1:tm, :]

    blk = pl.BlockSpec((tm, D_HALF), lambda i: (i, 0))
    return pl.pallas_call(
        body, name="fox_prep", grid=(s // tm,),
        in_specs=[pl.BlockSpec((tm, SEC), lambda i: (i, 0)), pl.BlockSpec((1, LANES), lambda i: (0, 0)),
                  pl.BlockSpec((1, D_HALF), lambda i: (0, 0)), pl.BlockSpec((1, D_HALF), lambda i: (0, 0))],
        out_specs=[blk, blk, blk, pl.BlockSpec((tm, LANES), lambda i: (i, 0)),
                   pl.BlockSpec((N_HEADS, tm), lambda i: (0, i))],
        out_shape=[jax.ShapeDtypeStruct((s, D_HALF), BF16)] * 3
                  + [jax.ShapeDtypeStruct((s, LANES), F32), jax.ShapeDtypeStruct((N_HEADS, s), F32)],
        scratch_shapes=[pltpu.VMEM((1, LANES), F32)],
        compiler_params=_params("arbitrary"),
    )(u_b, fb, qg, kg)


ATT_T = 256


def _attn_fwd(q, k, v, cc, cr, u_b):
    s = q.shape[0]
    t = ATT_T
    nblk = s // t

    def body(q_ref, k_ref, v_ref, cc_ref, cr_ref, g_ref, o_ref, mix_ref, lse_ref, m_sc, l_sc, acc_sc):
        i = pl.program_id(0)
        j = pl.program_id(1)

        @pl.when(j == 0)
        def _():
            m_sc[...] = jnp.full_like(m_sc, NEG)
            l_sc[...] = jnp.zeros_like(l_sc)
            acc_sc[...] = jnp.zeros_like(acc_sc)

        @pl.when(j <= i)
        def _():
            row = i * t + lax.broadcasted_iota(jnp.int32, (t, t), 0)
            col = j * t + lax.broadcasted_iota(jnp.int32, (t, t), 1)
            causal = row >= col
            left = lax.broadcasted_iota(jnp.int32, (1, LANES), 1) < HEAD
            for p in range(N_PAIRS):
                lanes = slice(p * LANES, (p + 1) * LANES)
                q2, k2, v2 = q_ref[:, lanes], k_ref[:, lanes], v_ref[:, lanes]
                acc2 = acc_sc[:, lanes]
                for e in range(2):
                    h = 2 * p + e
                    msk = left if e == 0 else jnp.logical_not(left)
                    sc = _dot_nt(jnp.where(msk, q2, jnp.zeros_like(q2)), k2)
                    sc = sc + (cc_ref[:, h:h + 1] - cr_ref[h:h + 1, :])
                    sc = jnp.where(causal, sc, NEG)
                    m_prev = m_sc[h]
                    m_new = jnp.maximum(m_prev, jnp.max(sc, axis=1, keepdims=True))
                    alpha = jnp.exp(m_prev - m_new)
                    pm = jnp.exp(sc - m_new)
                    l_sc[h] = alpha * l_sc[h] + jnp.sum(pm, axis=1, keepdims=True)
                    m_sc[h] = m_new
                    pv = jnp.dot(pm.astype(BF16), v2, preferred_element_type=F32)
                    acc2 = jnp.where(msk, alpha * acc2 + pv, acc2)
                acc_sc[:, lanes] = acc2

        @pl.when(j == i)
        def _():
            lane = lax.broadcasted_iota(jnp.int32, (1, LANES), 1)
            left = lane < HEAD
            lse = jnp.zeros((t, LANES), F32)
            for p in range(N_PAIRS):
                lanes = slice(p * LANES, (p + 1) * LANES)
                inv = jnp.where(left, 1.0 / l_sc[2 * p], 1.0 / l_sc[2 * p + 1])
                o = acc_sc[:, lanes] * inv
                o_ref[:, lanes] = o
                gate = g_ref[:, lanes]
                mix_ref[:, lanes] = o * (gate * _sigmoid(gate))
                for e in range(2):
                    h = 2 * p + e
                    lse = jnp.where(lane == h, m_sc[h] + jnp.log(l_sc[h]), lse)
            lse_ref[...] = lse

    qblk = pl.BlockSpec((t, D_HALF), lambda i, j: (i, 0))
    kblk = pl.BlockSpec((t, D_HALF), lambda i, j: (jnp.minimum(i, j), 0))
    return pl.pallas_call(
        body, name="fox_attn_fwd", grid=(nblk, nblk),
        in_specs=[qblk, kblk, kblk, pl.BlockSpec((t, LANES), lambda i, j: (i, 0)),
                  pl.BlockSpec((N_HEADS, t), lambda i, j: (0, jnp.minimum(i, j))),
                  pl.BlockSpec((t, D_HALF), lambda i, j: (i, 3))],
        out_specs=[qblk, qblk, pl.BlockSpec((t, LANES), lambda i, j: (i, 0))],
        out_shape=[jax.ShapeDtypeStruct((s, D_HALF), F32), jax.ShapeDtypeStruct((s, D_HALF), F32),
                   jax.ShapeDtypeStruct((s, LANES), F32)],
        scratch_shapes=[pltpu.VMEM((N_HEADS, t, 1), F32), pltpu.VMEM((N_HEADS, t, 1), F32),
                        pltpu.VMEM((t, D_HALF), F32)],
        compiler_params=_params("parallel", "arbitrary"),
    )(q, k, v, cc, cr, u_b)


def _fox_post_bwd(dmix, o, u_b, tm=256):
    s = o.shape[0]

    def body(dm_ref, o_ref, g_ref, do_ref, dg_ref):
        gate = g_ref[...]
        sg = _sigmoid(gate)
        dm = dm_ref[...]
        do_ref[...] = (dm * (gate * sg)).astype(BF16)
        dg_ref[...] = dm * o_ref[...] * (sg * (1.0 + gate * (1.0 - sg)))

    blk = pl.BlockSpec((tm, D_HALF), lambda i: (i, 0))
    return pl.pallas_call(
        body, name="fox_post_bwd", grid=(s // tm,),
        in_specs=[blk, blk, pl.BlockSpec((tm, D_HALF), lambda i: (i, 3))], out_specs=[blk] * 2,
        out_shape=[jax.ShapeDtypeStruct((s, D_HALF), BF16), jax.ShapeDtypeStruct((s, D_HALF), F32)],
        compiler_params=_params("parallel"),
    )(dmix, o, u_b)


def _attn_probs(q2, k2, v2, do2, msk, causal, bias, lse_col):
    zero = jnp.zeros_like(q2)
    qh = jnp.where(msk, q2, zero)
    doh = jnp.where(msk, do2, zero)
    sc = jnp.where(causal, _dot_nt(qh, k2) + bias, NEG)
    pm = jnp.exp(sc - lse_col)
    dp = _dot_nt(doh, v2)
    return qh, doh, pm, dp


def _attn_bwd_rowdot(q, k, v, do, lse, cc, cr):
    s = q.shape[0]
    t = ATT_T
    nblk = s // t

    def body(q_ref, k_ref, v_ref, do_ref, lse_ref, cc_ref, cr_ref, dd_ref, acc):
        i = pl.program_id(0)
        j = pl.program_id(1)

        @pl.when(j == 0)
        def _():
            acc[...] = jnp.zeros_like(acc)

        @pl.when(j <= i)
        def _():
            row = i * t + lax.broadcasted_iota(jnp.int32, (t, t), 0)
            col = j * t + lax.broadcasted_iota(jnp.int32, (t, t), 1)
            causal = row >= col
            left = lax.broadcasted_iota(jnp.int32, (1, LANES), 1) < HEAD
            for p in range(N_PAIRS):
                lanes = slice(p * LANES, (p + 1) * LANES)
                q2, k2, v2, do2 = q_ref[:, lanes], k_ref[:, lanes], v_ref[:, lanes], do_ref[:, lanes]
                for e in range(2):
                    h = 2 * p + e
                    msk = left if e == 0 else jnp.logical_not(left)
                    bias = cc_ref[:, h:h + 1] - cr_ref[h:h + 1, :]
                    _, _, pm, dp = _attn_probs(q2, k2, v2, do2, msk, causal, bias, lse_ref[:, h:h + 1])
                    acc[h] += jnp.sum(pm * dp, axis=1, keepdims=True)

        @pl.when(j == i)
        def _():
            lane = lax.broadcasted_iota(jnp.int32, (1, LANES), 1)
            dd = jnp.zeros((t, LANES), F32)
            for h in range(N_HEADS):
                dd = jnp.where(lane == h, acc[h], dd)
            dd_ref[...] = dd

    qblk = pl.BlockSpec((t, D_HALF), lambda i, j: (i, 0))
    qcol = pl.BlockSpec((t, LANES), lambda i, j: (i, 0))
    kblk = pl.BlockSpec((t, D_HALF), lambda i, j: (jnp.minimum(i, j), 0))
    return pl.pallas_call(
        body, name="fox_attn_rowdot", grid=(nblk, nblk),
        in_specs=[qblk, kblk, kblk, qblk, qcol, qcol, pl.BlockSpec((N_HEADS, t), lambda i, j: (0, jnp.minimum(i, j)))],
        out_specs=qcol, out_shape=jax.ShapeDtypeStruct((s, LANES), F32),
        scratch_shapes=[pltpu.VMEM((N_HEADS, t, 1), F32)],
        compiler_params=_params("parallel", "arbitrary"),
    )(q, k, v, do, lse, cc, cr)


def _attn_bwd(q, k, v, do, lse, dd, cc, cr):
    s = q.shape[0]
    t = ATT_T
    nblk = s // t

    def body(q_ref, k_ref, v_ref, do_ref, lse_ref, dd_ref, cc_ref, cr_ref,
             dq_ref, dk_ref, dv_ref, dcr_ref, dk_sc, dv_sc, dcr_sc):
        j = pl.program_id(0)
        i = pl.program_id(1)

        @pl.when(jnp.logical_and(j == 0, i == 0))
        def _():
            dq_ref[...] = jnp.zeros_like(dq_ref)

        @pl.when(i == 0)
        def _():
            dk_sc[...] = jnp.zeros_like(dk_sc)
            dv_sc[...] = jnp.zeros_like(dv_sc)
            dcr_sc[...] = jnp.zeros_like(dcr_sc)

        @pl.when(i >= j)
        def _():
            row = i * t + lax.broadcasted_iota(jnp.int32, (t, t), 0)
            col = j * t + lax.broadcasted_iota(jnp.int32, (t, t), 1)
            causal = row >= col
            left = lax.broadcasted_iota(jnp.int32, (1, LANES), 1) < HEAD
            qrows = pl.ds(pl.multiple_of(i * t, t), t)
            for p in range(N_PAIRS):
                lanes = slice(p * LANES, (p + 1) * LANES)
                q2, k2, v2, do2 = q_ref[:, lanes], k_ref[:, lanes], v_ref[:, lanes], do_ref[:, lanes]
                zero = jnp.zeros_like(q2)
                dq2 = jnp.zeros((t, LANES), F32)
                dk2 = jnp.zeros((t, LANES), F32)
                dv2 = jnp.zeros((t, LANES), F32)
                for e in range(2):
                    h = 2 * p + e
                    msk = left if e == 0 else jnp.logical_not(left)
                    bias = cc_ref[:, h:h + 1] - cr_ref[h:h + 1, :]
                    qh, doh, pm, dp = _attn_probs(q2, k2, v2, do2, msk, causal, bias, lse_ref[:, h:h + 1])
                    dsc = pm * (dp - dd_ref[:, h:h + 1])
                    dsb = dsc.astype(BF16)
                    dv2 += _dot_tn(pm.astype(BF16), doh)
                    dk2 += _dot_tn(dsb, qh)
                    dq2 += jnp.dot(dsb, jnp.where(msk, k2, zero), preferred_element_type=F32)
                    dcr_sc[h:h + 1, :] += -_colsum(dsc)
                dq_ref[qrows, lanes] += dq2 * ATT_SCALE
                dk_sc[:, lanes] += dk2
                dv_sc[:, lanes] += dv2

        @pl.when(i == nblk - 1)
        def _():
            dk_ref[...] = dk_sc[...]
            dv_ref[...] = dv_sc[...]
            dcr_ref[...] = dcr_sc[...]

    qblk = pl.BlockSpec((t, D_HALF), lambda j, i: (jnp.maximum(i, j), 0))
    qcol = pl.BlockSpec((t, LANES), lambda j, i: (jnp.maximum(i, j), 0))
    kblk = pl.BlockSpec((t, D_HALF), lambda j, i: (j, 0))
    return pl.pallas_call(
        body, name="fox_attn_bwd", grid=(nblk, nblk),
        in_specs=[qblk, kblk, kblk, qblk, qcol, qcol, qcol, pl.BlockSpec((N_HEADS, t), lambda j, i: (0, j))],
        out_specs=[pl.BlockSpec((s, D_HALF), lambda j, i: (0, 0)), kblk, kblk,
                   pl.BlockSpec((N_HEADS, t), lambda j, i: (0, j))],
        out_shape=[jax.ShapeDtypeStruct((s, D_HALF), F32)] * 3 + [jax.ShapeDtypeStruct((N_HEADS, s), F32)],
        scratch_shapes=[pltpu.VMEM((t, D_HALF), F32), pltpu.VMEM((t, D_HALF), F32), pltpu.VMEM((N_HEADS, t), F32)],
        compiler_params=_params("arbitrary", "arbitrary"),
    )(q, k, v, do, lse, dd, cc, cr)


def _fox_prep_bwd(u_b, dq, dk, dv, dgate, dcum, fb, qg, kg, tm=256):
    s = u_b.shape[0]
    nb = s // tm

    def body(ub_ref, dq_ref, dk_ref, dv_ref, dg_ref, dc_ref, fb_ref, qg_ref, kg_ref,
             du_ref, dqg_ref, dkg_ref, dfb_ref, carry):
        i = pl.program_id(0)

        @pl.when(i == 0)
        def _():
            carry[...] = jnp.zeros_like(carry)
            dqg_ref[...] = jnp.zeros_like(dqg_ref)
            dkg_ref[...] = jnp.zeros_like(dkg_ref)
            dfb_ref[...] = jnp.zeros_like(dfb_ref)

        bd = _head_ones()
        for lo, g_ref, d_ref, dgain_ref in ((0, qg_ref, dq_ref, dqg_ref), (512, kg_ref, dk_ref, dkg_ref)):
            gain = g_ref[...]
            xh, rinv, _ = _head_rms(ub_ref[:, lo:lo + 512], gain, bd)
            dn = d_ref[...]
            dgain_ref[...] += _colsum(dn * xh)
            dxh = dn * gain
            du_ref[:, lo:lo + 512] = rinv * (dxh - xh * (_head_sum(dxh * xh, bd) * (1.0 / HEAD)))
        du_ref[:, 1024:1536] = dv_ref[...]
        du_ref[:, 1536:2048] = dg_ref[...]
        lane = lax.broadcasted_iota(jnp.int32, (1, LANES), 1)
        dc = dc_ref[...]
        dlogf = _exact_dot(dc, _tri(tm, False), ones_first=True) + carry[...]
        carry[...] += _colsum(dc)
        fl = ub_ref[:, 2048:2176] + fb_ref[...]
        dfl = jnp.where(lane < N_HEADS, dlogf * (1.0 - _sigmoid(fl)), 0.0)
        du_ref[:, 2048:2176] = dfl
        dfb_ref[...] += _colsum(dfl)

    rev = lambda w: pl.BlockSpec((tm, w), lambda i: (nb - 1 - i, 0))
    vec = lambda w: pl.BlockSpec((1, w), lambda i: (0, 0))
    return pl.pallas_call(
        body, name="fox_prep_bwd", grid=(nb,),
        in_specs=[rev(SEC)] + [rev(D_HALF)] * 4 + [rev(LANES), vec(LANES), vec(D_HALF), vec(D_HALF)],
        out_specs=[rev(SEC), vec(D_HALF), vec(D_HALF), vec(LANES)],
        out_shape=[jax.ShapeDtypeStruct((s, SEC), F32), jax.ShapeDtypeStruct((1, D_HALF), F32),
                   jax.ShapeDtypeStruct((1, D_HALF), F32), jax.ShapeDtypeStruct((1, LANES), F32)],
        scratch_shapes=[pltpu.VMEM((1, LANES), F32)],
        compiler_params=_params("arbitrary"),
    )(u_b, dq, dk, dv, dgate, dcum, fb, qg, kg)


def _merge(mix_a, mix_b, u_g, x, tgt, wa, wb, wo, fg, tm=256):
    s, d = x.shape

    def body(ma_ref, mb_ref, ug_ref, x_ref, t_ref, wa_ref, wb_ref, wo_ref, fg_ref,
             dx2_ref, dma_ref, dmb_ref, dug_ref, dwa_ref, dwb_ref, dwo_ref, dfg_ref, loss_ref):
        i = pl.program_id(0)

        @pl.when(i == 0)
        def _():
            for ref in (dwa_ref, dwb_ref, dwo_ref, dfg_ref, loss_ref):
                ref[...] = jnp.zeros_like(ref)

        wa_v, wb_v, wo_v, fg_v = wa_ref[...], wb_ref[...], wo_ref[...], fg_ref[...]
        ma = ma_ref[...].astype(BF16)
        mb = mb_ref[...].astype(BF16)
        ya = jnp.dot(ma, wa_v, preferred_element_type=F32)
        yb = jnp.dot(mb, wb_v, preferred_element_type=F32)
        sa = _sigmoid(ug_ref[:, 0:d])
        sb = _sigmoid(ug_ref[:, d:2 * d])
        merged = (sa * ya + sb * yb).astype(BF16)
        x2 = x_ref[...] + jnp.dot(merged, wo_v, preferred_element_type=F32)
        r2 = lax.rsqrt(jnp.mean(x2 * x2, axis=-1, keepdims=True) + RMS_EPS)
        x2h = x2 * r2
        err = x2h * fg_v - t_ref[...]
        loss_ref[...] += _colsum(err * err)
        dy = err * (1.0 / d)
        dfg_ref[...] += _colsum(dy * x2h)
        dx2h = dy * fg_v
        dx2 = r2 * (dx2h - x2h * jnp.mean(dx2h * x2h, axis=-1, keepdims=True))
        dx2_ref[...] = dx2
        dx2b = dx2.astype(BF16)
        dmerged = _dot_nt(dx2b, wo_v)
        dwo_ref[...] += _dot_tn(merged, dx2b)
        dya = dmerged * sa
        dyb = dmerged * sb
        dug_ref[:, 0:d] = dya * ya * (1.0 - sa)
        dug_ref[:, d:2 * d] = dyb * yb * (1.0 - sb)
        dyab = dya.astype(BF16)
        dybb = dyb.astype(BF16)
        dma_ref[...] = _dot_nt(dyab, wa_v)
        dmb_ref[...] = _dot_nt(dybb, wb_v)
        dwa_ref[...] += _dot_tn(ma, dyab)
        dwb_ref[...] += _dot_tn(mb, dybb)

    row = lambda w: pl.BlockSpec((tm, w), lambda i: (i, 0))
    full = lambda a: pl.BlockSpec(a.shape, lambda i: (0, 0))
    fshape = lambda a: jax.ShapeDtypeStruct(a.shape, F32)
    return pl.pallas_call(
        body, name="merge_fwd_bwd", grid=(s // tm,),
        in_specs=[row(D_HALF), row(D_HALF), row(GATE_COLS), row(d), row(d), full(wa), full(wb), full(wo), full(fg)],
        out_specs=[row(d), row(D_HALF), row(D_HALF), row(GATE_COLS), full(wa), full(wb), full(wo), full(fg), full(fg)],
        out_shape=[jax.ShapeDtypeStruct((s, d), F32), jax.ShapeDtypeStruct((s, D_HALF), F32),
                   jax.ShapeDtypeStruct((s, D_HALF), F32), jax.ShapeDtypeStruct((s, GATE_COLS), F32),
                   fshape(wa), fshape(wb), fshape(wo), fshape(fg), fshape(fg)],
        compiler_params=_params("arbitrary"),
    )(mix_a, mix_b, u_g, x, tgt, wa, wb, wo, fg)


def _lora_weight(w_up, a_up):
    z = jnp.zeros((LORA, D_HALF), w_up.dtype)
    return jnp.concatenate([jnp.concatenate([w_up, z], axis=1), jnp.concatenate([z, a_up], axis=1)], axis=0)


def _device_grads(x, tgt, norm_g, w_a, w_b, w_g, shift_mu, w_up, w0, a_up, a0, k_k, k_a, r_k, lnx_w, lnx_b,
                  f_bias, q_norm_g, k_norm_g, w_out_a, w_out_b, w_out, final_norm_g):
    wl = _lora_weight(w_up, a_up)
    rk = r_k.reshape(1, D_HALF)
    fb = jnp.pad(f_bias, ((0, 0), (0, LANES - N_HEADS)))
    qg = jnp.tile(q_norm_g, (1, N_HEADS))
    kg = jnp.tile(k_norm_g, (1, N_HEADS))
    fg = final_norm_g.reshape(1, D_MODEL)

    h = _rmsnorm_in(x, norm_g)
    u_a = _matmul_nn(h, w_a, "inproj_rwkv")
    u_b = _matmul_nn(h, w_b, "inproj_fox")
    u_g = _matmul_nn(h, w_g, "inproj_gate")

    r, dec, k, v, av, bv, gate_a = _rwkv_prep(u_a, shift_mu, wl, w0, a0, k_k, k_a)
    y, st = _wkv_fwd(r, dec, k, av, bv, v)
    mix_a = _rwkv_post(y, r, k, v, gate_a, lnx_w, lnx_b, rk)

    q, kn, vb, cc, cr = _fox_prep(u_b, fb, qg, kg)
    o, mix_b, lse = _attn_fwd(q, kn, vb, cc, cr, u_b)

    dx2, dmix_a, dmix_b, du_g, dwa, dwb, dwo, dfg, loss_vec = _merge(
        mix_a, mix_b, u_g, x, tgt, w_out_a, w_out_b, w_out, fg)

    do, dgate_b = _fox_post_bwd(dmix_b, o, u_b)
    dd = _attn_bwd_rowdot(q, kn, vb, do, lse, cc, cr)
    dq, dk_att, dv_att, dcr = _attn_bwd(q, kn, vb, do, lse, dd, cc, cr)
    dcum = jnp.pad(dcr.T, ((0, 0), (0, LANES - N_HEADS)))
    du_b, dqg, dkg, dfb = _fox_prep_bwd(u_b, dq, dk_att, dv_att, dgate_b, dcum, fb, qg, kg)

    dy, dr_b, dk_b, dv_b, dgate_a, dlw, dlb, drk = _rwkv_post_bwd(dmix_a, y, r, k, v, gate_a, lnx_w, lnx_b, rk)
    dr_s, dw_s, dk_s, dv_s, da_s, db_s = _wkv_bwd(r, dec, k, av, bv, v, dy, st)
    du_a, dmu, dwl, dw0, da0, dkkw, dkaw = _rwkv_prep_bwd(
        u_a, (dr_s, dw_s, dk_s, dv_s, da_s, db_s, dr_b, dk_b, dv_b, dgate_a), shift_mu, wl, w0, a0, k_k, k_a)

    h_t = h.T
    dw_a = _matmul_tn_acc(h_t, du_a, "dw_rwkv")
    dw_b = _matmul_tn_acc(h_t, du_b, "dw_fox")
    dw_g = _matmul_tn_acc(h_t, du_g, "dw_gate")
    grad_x, dnorm_g = _inproj_bwd(du_a, du_b, du_g, w_a, w_b, w_g, x, dx2, norm_g)

    grads = dict(
        norm_g=dnorm_g, w_in=jnp.concatenate([dw_a, dw_b[:, :FOX_REAL], dw_g], axis=1), shift_mu=dmu,
        w_lora_up=dwl[:LORA, :D_HALF], w0=dw0, a_lora_up=dwl[LORA:, D_HALF:], a0=da0, k_k=dkkw, k_a=dkaw,
        r_k=drk.reshape(1, N_HEADS, HEAD), lnx_w=dlw, lnx_b=dlb, f_bias=dfb[:, :N_HEADS],
        q_norm_g=dqg.reshape(N_HEADS, HEAD).sum(axis=0, keepdims=True),
        k_norm_g=dkg.reshape(N_HEADS, HEAD).sum(axis=0, keepdims=True),
        w_out_a=dwa, w_out_b=dwb, w_out=dwo, final_norm_g=dfg.reshape(D_MODEL))
    return loss_vec, grad_x, grads


CHIP_FLIPS = ((1, 0), (0, 1), (1, 1))
ANY = pl.BlockSpec(memory_space=pl.ANY)


def _position():
    return lax.axis_index("x"), lax.axis_index("y"), lax.axis_index("c")


def _flip(v, f):
    return 1 - v if f else v


def _gather_shards(shards):
    n = len(shards)
    n_remote = len(CHIP_FLIPS) * n

    def body(*refs):
        ins, outs = refs[:n], refs[n:2 * n]
        send_sems, recv_sems, local_sems = refs[2 * n:]
        x, y, c = _position()
        me = 2 * x + y
        local = [pltpu.make_async_copy(ins[t], outs[t].at[me], local_sems.at[t]) for t in range(n)]
        for cp in local:
            cp.start()
        sends, recvs = [], []
        for f, (fx, fy) in enumerate(CHIP_FLIPS):
            px, py = _flip(x, fx), _flip(y, fy)
            for t in range(n):
                sems = dict(send_sem=send_sems.at[f * n + t], recv_sem=recv_sems.at[f * n + t],
                            device_id=(px, py, c), device_id_type=MESH)
                sends.append(pltpu.make_async_remote_copy(src_ref=ins[t], dst_ref=outs[t].at[me], **sems))
                recvs.append(pltpu.make_async_remote_copy(src_ref=ins[t], dst_ref=outs[t].at[2 * px + py], **sems))
        for cp in sends:
            cp.start()
        for cp in recvs:
            cp.wait_recv()
        for cp in sends:
            cp.wait_send()
        for cp in local:
            cp.wait()

    return pl.pallas_call(
        body, name="gather_weights", in_specs=[ANY] * n, out_specs=[ANY] * n,
        out_shape=[jax.ShapeDtypeStruct((N_CHIPS,) + a.shape, a.dtype) for a in shards],
        scratch_shapes=[pltpu.SemaphoreType.DMA((n_remote,)), pltpu.SemaphoreType.DMA((n_remote,)),
                        pltpu.SemaphoreType.DMA((n,))],
        compiler_params=pltpu.CompilerParams(has_side_effects=True),
    )(*shards)


def _scatter_partials(stacks):
    n = len(stacks)
    n_remote = len(CHIP_FLIPS) * n

    def body(*refs):
        ins, outs = refs[:n], refs[n:2 * n]
        send_sems, recv_sems = refs[2 * n:]
        x, y, c = _position()
        sends, recvs = [], []
        for f, (fx, fy) in enumerate(CHIP_FLIPS):
            px, py = _flip(x, fx), _flip(y, fy)
            for t in range(n):
                cp = pltpu.make_async_remote_copy(
                    src_ref=ins[t].at[2 * px + py], dst_ref=outs[t].at[f],
                    send_sem=send_sems.at[f * n + t], recv_sem=recv_sems.at[f * n + t],
                    device_id=(px, py, c), device_id_type=MESH)
                sends.append(cp)
        for cp in sends:
            cp.start()
        for cp in sends:
            cp.wait_recv()
        for cp in sends:
            cp.wait_send()

    return pl.pallas_call(
        body, name="scatter_partials", in_specs=[ANY] * n, out_specs=[ANY] * n,
        out_shape=[jax.ShapeDtypeStruct((len(CHIP_FLIPS),) + a.shape[1:], a.dtype) for a in stacks],
        scratch_shapes=[pltpu.SemaphoreType.DMA((n_remote,)), pltpu.SemaphoreType.DMA((n_remote,))],
        compiler_params=pltpu.CompilerParams(has_side_effects=True),
    )(*stacks)


def _swap_sibling(tensors):
    n = len(tensors)

    def body(*refs):
        ins, outs = refs[:n], refs[n:2 * n]
        send_sems, recv_sems = refs[2 * n:]
        x, y, c = _position()
        copies = [pltpu.make_async_remote_copy(
            src_ref=ins[t], dst_ref=outs[t], send_sem=send_sems.at[t], recv_sem=recv_sems.at[t],
            device_id=(x, y, 1 - c), device_id_type=MESH) for t in range(n)]
        for cp in copies:
            cp.start()
        for cp in copies:
            cp.wait_recv()
        for cp in copies:
            cp.wait_send()

    return pl.pallas_call(
        body, name="swap_sibling", in_specs=[ANY] * n, out_specs=[ANY] * n,
        out_shape=[jax.ShapeDtypeStruct(a.shape, a.dtype) for a in tensors],
        scratch_shapes=[pltpu.SemaphoreType.DMA((n,)), pltpu.SemaphoreType.DMA((n,))],
        compiler_params=pltpu.CompilerParams(has_side_effects=True),
    )(*tensors)


def _allreduce_small(slab):
    stages = 3

    def body(x_ref, o_ref, buf, send_sems, recv_sems):
        x, y, c = _position()
        peers = ((1 - x, y, c), (x, 1 - y, c), (x, y, 1 - c))
        o_ref[...] = x_ref[...]
        for k, peer in enumerate(peers):
            cp = pltpu.make_async_remote_copy(src_ref=o_ref, dst_ref=buf.at[k], send_sem=send_sems.at[k],
                                              recv_sem=recv_sems.at[k], device_id=peer, device_id_type=MESH)
            cp.start()
            cp.wait()
            o_ref[...] = o_ref[...] + buf[k]

    return pl.pallas_call(
        body, name="allreduce_small",
        in_specs=[pl.BlockSpec(memory_space=pltpu.VMEM)], out_specs=pl.BlockSpec(memory_space=pltpu.VMEM),
        out_shape=jax.ShapeDtypeStruct(slab.shape, slab.dtype),
        scratch_shapes=[pltpu.VMEM((stages,) + slab.shape, slab.dtype),
                        pltpu.SemaphoreType.DMA((stages,)), pltpu.SemaphoreType.DMA((stages,))],
        compiler_params=pltpu.CompilerParams(has_side_effects=True),
    )(slab)


def _row_tile(r):
    return min(r, 256)


def _sum4(stack, recv, me):
    _, r, c = stack.shape
    tr = _row_tile(r)

    def body(me_ref, own_ref, recv_ref, o_ref):
        o_ref[...] = (((own_ref[...] + recv_ref[0].astype(F32)) + recv_ref[1].astype(F32))
                      + recv_ref[2].astype(F32))

    return pl.pallas_call(
        body, name="sum_partials",
        grid_spec=pltpu.PrefetchScalarGridSpec(
            num_scalar_prefetch=1, grid=(r // tr,),
            in_specs=[pl.BlockSpec((None, tr, c), lambda i, me_ref: (me_ref[0], i, 0)),
                      pl.BlockSpec((len(CHIP_FLIPS), tr, c), lambda i, me_ref: (0, i, 0))],
            out_specs=pl.BlockSpec((tr, c), lambda i, me_ref: (i, 0))),
        out_shape=jax.ShapeDtypeStruct((r, c), F32), compiler_params=_params("parallel"),
    )(me, stack, recv)


def _adamw_math(w, g, m, v):
    m = ADAM_B1 * m + (1.0 - ADAM_B1) * g
    v = ADAM_B2 * v + (1.0 - ADAM_B2) * (g * g)
    m_hat = m / (1.0 - ADAM_B1 ** ADAM_STEP)
    v_hat = v / (1.0 - ADAM_B2 ** ADAM_STEP)
    delta = -ADAM_LR * (m_hat / (jnp.sqrt(v_hat) + ADAM_EPS) + ADAM_WD * w)
    return delta, m, v


def _adamw(w, m, v, g_parts, name):
    r, c = w.shape
    tr = _row_tile(r)
    n = len(g_parts)

    def body(*refs):
        w_ref, m_ref, v_ref = refs[:3]
        g_refs = refs[3:3 + n]
        g_out, d_out, m_out, v_out = refs[3 + n:]
        g = g_refs[0][...]
        for ref in g_refs[1:]:
            g = g + ref[...]
        g_out[...] = g
        d_out[...], m_out[...], v_out[...] = _adamw_math(w_ref[...], g, m_ref[...], v_ref[...])

    blk = pl.BlockSpec((tr, c), lambda i: (i, 0))
    return pl.pallas_call(
        body, name=name, grid=(r // tr,), in_specs=[blk] * (3 + n), out_specs=[blk] * 4,
        out_shape=[jax.ShapeDtypeStruct((r, c), F32)] * 4, compiler_params=_params("parallel"),
    )(w, m, v, *g_parts)


SHARDED = ("w_in", "w_lora_up", "a_lora_up", "w_out_a", "w_out_b", "w_out")
ROW_SHARDED = ("w_out",)
SMALL = ("norm_g", "shift_mu", "w0", "a0", "k_k", "k_a", "r_k", "lnx_w", "lnx_b", "f_bias", "q_norm_g", "k_norm_g",
         "final_norm_g")
WEIGHTS = ("norm_g", "w_in", "shift_mu", "w_lora_up", "w0", "a_lora_up", "a0", "k_k", "k_a", "r_k", "lnx_w", "lnx_b",
           "f_bias", "q_norm_g", "k_norm_g", "w_out_a", "w_out_b", "w_out", "final_norm_g")
SLAB_ROWS = 16
SLAB_COLS = SEC


def _to_slab(named, extra=None):
    rows = [jnp.pad(named[n].reshape(1, -1), ((0, 0), (0, SLAB_COLS - named[n].size))) for n in SMALL]
    if extra is not None:
        rows.append(jnp.pad(extra.reshape(1, -1), ((0, 0), (0, SLAB_COLS - extra.size))))
    rows.append(jnp.zeros((SLAB_ROWS - len(rows), SLAB_COLS), F32))
    return jnp.concatenate(rows, axis=0)


def _from_slab(slab, shapes):
    return {n: slab[i, :math.prod(shapes[n])].reshape(shapes[n]) for i, n in enumerate(SMALL)}


def _by_chip(g, name):
    if name in ROW_SHARDED:
        return g.reshape(N_CHIPS, g.shape[0] // N_CHIPS, g.shape[1])
    r, c = g.shape
    return g.reshape(r, N_CHIPS, c // N_CHIPS).transpose(1, 0, 2)


def _from_chips(stack, name):
    if name in ROW_SHARDED:
        return stack.reshape(-1, stack.shape[2])
    _, r, c = stack.shape
    return stack.transpose(1, 0, 2).reshape(r, N_CHIPS * c)


def kernel(x, norm_g, w_in, shift_mu, w_lora_up, w0, a_lora_up, a0, k_k, k_a, r_k, lnx_w, lnx_b, f_bias, q_norm_g, k_norm_g, w_out_a, w_out_b, w_out, final_norm_g, loss_target, m_norm_g, m_w_in, m_shift_mu, m_w_lora_up, m_w0, m_a_lora_up, m_a0, m_k_k, m_k_a, m_r_k, m_lnx_w, m_lnx_b, m_f_bias, m_q_norm_g, m_k_norm_g, m_w_out_a, m_w_out_b, m_w_out, m_final_norm_g, v_norm_g, v_w_in, v_shift_mu, v_w_lora_up, v_w0, v_a_lora_up, v_a0, v_k_k, v_k_a, v_r_k, v_lnx_w, v_lnx_b, v_f_bias, v_q_norm_g, v_k_norm_g, v_w_out_a, v_w_out_b, v_w_out, v_final_norm_g):
    w = dict(norm_g=norm_g, w_in=w_in, shift_mu=shift_mu, w_lora_up=w_lora_up, w0=w0, a_lora_up=a_lora_up, a0=a0,
             k_k=k_k, k_a=k_a, r_k=r_k, lnx_w=lnx_w, lnx_b=lnx_b, f_bias=f_bias, q_norm_g=q_norm_g,
             k_norm_g=k_norm_g, w_out_a=w_out_a, w_out_b=w_out_b, w_out=w_out, final_norm_g=final_norm_g)
    m = dict(norm_g=m_norm_g, w_in=m_w_in, shift_mu=m_shift_mu, w_lora_up=m_w_lora_up, w0=m_w0,
             a_lora_up=m_a_lora_up, a0=m_a0, k_k=m_k_k, k_a=m_k_a, r_k=m_r_k, lnx_w=m_lnx_w, lnx_b=m_lnx_b,
             f_bias=m_f_bias, q_norm_g=m_q_norm_g, k_norm_g=m_k_norm_g, w_out_a=m_w_out_a, w_out_b=m_w_out_b,
             w_out=m_w_out, final_norm_g=m_final_norm_g)
    v = dict(norm_g=v_norm_g, w_in=v_w_in, shift_mu=v_shift_mu, w_lora_up=v_w_lora_up, w0=v_w0,
             a_lora_up=v_a_lora_up, a0=v_a0, k_k=v_k_k, k_a=v_k_a, r_k=v_r_k, lnx_w=v_lnx_w, lnx_b=v_lnx_b,
             f_bias=v_f_bias, q_norm_g=v_q_norm_g, k_norm_g=v_k_norm_g, w_out_a=v_w_out_a, w_out_b=v_w_out_b,
             w_out=v_w_out, final_norm_g=v_final_norm_g)
    shapes = {n: w[n].shape for n in WEIGHTS}

    gathered = _gather_shards([w[n][0].astype(BF16) for n in SHARDED])
    full = {n: _from_chips(g, n) for n, g in zip(SHARDED, gathered)}
    w_full = full["w_in"]
    w_a = w_full[:, :RWKV_COLS]
    w_b = jnp.pad(w_full[:, RWKV_COLS:RWKV_COLS + FOX_REAL], ((0, 0), (0, SEC - FOX_REAL)))
    w_g = w_full[:, RWKV_COLS + FOX_REAL:]

    loss_vec, grad_x, grads = _device_grads(
        x[0], loss_target[0], norm_g, w_a, w_b, w_g, shift_mu, full["w_lora_up"], w0, full["a_lora_up"], a0, k_k, k_a,
        r_k, lnx_w, lnx_b, f_bias, q_norm_g, k_norm_g, full["w_out_a"], full["w_out_b"], full["w_out"], final_norm_g)

    total = _allreduce_small(_to_slab(grads, extra=loss_vec))
    loss = (0.5 / D_MODEL) * jnp.sum(total[len(SMALL)])
    slab_g, slab_d, slab_m, slab_v = _adamw(_to_slab(w), _to_slab(m), _to_slab(v), [total], "adamw_small")
    out_g, out_d, out_m, out_v = (_from_slab(s, shapes) for s in (total, slab_d, slab_m, slab_v))
    del slab_g

    xpos, ypos, _ = _position()
    me = (2 * xpos + ypos).astype(jnp.int32).reshape(1)
    stacks = [_by_chip(grads[n], n) for n in SHARDED]
    received = _scatter_partials([s.astype(BF16) for s in stacks])
    core_sums = [_sum4(s, r, me) for s, r in zip(stacks, received)]
    sibling_sums = _swap_sibling(core_sums)
    for n, mine, theirs in zip(SHARDED, core_sums, sibling_sums):
        g, d, m2, v2 = _adamw(w[n][0], m[n][0], v[n][0], [mine, theirs], "adamw_" + n)
        out_g[n], out_d[n], out_m[n], out_v[n] = (a.reshape(shapes[n]) for a in (g, d, m2, v2))

    return (loss, grad_x.reshape(x.shape), *[out_g[n] for n in WEIGHTS], *[out_d[n] for n in WEIGHTS],
            *[out_m[n] for n in WEIGHTS], *[out_v[n] for n in WEIGHTS])
```

```python
import functools
import math

import jax
import jax.numpy as jnp
from jax import lax
from jax.experimental import pallas as pl
from jax.experimental.pallas import tpu as pltpu

F32 = jnp.float32
BF16 = jnp.bfloat16

D_MODEL = 1024
D_HALF = 512
HEAD = 64
N_HEADS = 8
LORA = 64
RWKV_COLS = 2176
FOX_REAL = 2056
SEC = 2176
GATE_COLS = 2048
IN_COLS = 6280
N_CHIPS = 4
SHARD_COLS = IN_COLS // N_CHIPS
A_TAIL = RWKV_COLS - SHARD_COLS
B_HEAD = SHARD_COLS - A_TAIL
B_TAIL = FOX_REAL - B_HEAD
G_HEAD = SHARD_COLS - B_TAIL
RMS_EPS = 1e-6
LNX_EPS = 64e-5
ATT_SCALE = HEAD ** -0.5
NEG = -1e30

ADAM_LR = 0.001
ADAM_B1 = 0.9
ADAM_B2 = 0.999
ADAM_EPS = 1e-08
ADAM_WD = 0.01
ADAM_STEP = 10

LANES = 128
SUBLANES = 8
VMEM_LIMIT = 56 * 1024 * 1024
MESH = pl.DeviceIdType.MESH


def _params(*sem):
    return pltpu.CompilerParams(dimension_semantics=sem if sem else None, vmem_limit_bytes=VMEM_LIMIT)


def _sigmoid(x):
    return 1.0 / (1.0 + jnp.exp(-x))


def _log_sigmoid(x):
    return jnp.minimum(x, 0.0) - jnp.log(1.0 + jnp.exp(-jnp.abs(x)))


def _head_ones():
    r = lax.broadcasted_iota(jnp.int32, (LANES, LANES), 0) >> 6
    c = lax.broadcasted_iota(jnp.int32, (LANES, LANES), 1) >> 6
    return (r == c).astype(BF16)


def _split3(x):
    hi = x.astype(BF16)
    r1 = x - hi.astype(F32)
    mid = r1.astype(BF16)
    lo = (r1 - mid.astype(F32)).astype(BF16)
    return hi, mid, lo


def _exact_dot(x, ones_bf16, ones_first=False):
    out = None
    for piece in _split3(x):
        if ones_first:
            t = jnp.dot(ones_bf16, piece, preferred_element_type=F32)
        else:
            t = jnp.dot(piece, ones_bf16, preferred_element_type=F32)
        out = t if out is None else out + t
    return out


def _head_sum(x, bd):
    n = x.shape[1] // LANES
    parts = [_exact_dot(x[:, i * LANES:(i + 1) * LANES], bd) for i in range(n)]
    return parts[0] if n == 1 else jnp.concatenate(parts, axis=1)


def _dot_nt(a, b):
    return lax.dot_general(a, b, (((1,), (1,)), ((), ())), preferred_element_type=F32)


def _dot_tn(a, b):
    return lax.dot_general(a, b, (((0,), (0,)), ((), ())), preferred_element_type=F32)


def _colsum(x):
    return jnp.sum(x, axis=0, keepdims=True)


def _rmsnorm_in(x, g, tm=512):
    s, d = x.shape

    def body(x_ref, g_ref, h_ref):
        xv = x_ref[...]
        r = lax.rsqrt(jnp.mean(xv * xv, axis=-1, keepdims=True) + RMS_EPS)
        h_ref[...] = (xv * r * g_ref[...]).astype(BF16)

    return pl.pallas_call(
        body, name="rmsnorm_in", grid=(s // tm,),
        in_specs=[pl.BlockSpec((tm, d), lambda i: (i, 0)), pl.BlockSpec((1, d), lambda i: (0, 0))],
        out_specs=pl.BlockSpec((tm, d), lambda i: (i, 0)),
        out_shape=jax.ShapeDtypeStruct((s, d), BF16), compiler_params=_params("parallel"),
    )(x, g)


def _matmul_nn(a, b, name, tm=512):
    m, k = a.shape
    n = b.shape[1]

    def body(a_ref, b_ref, o_ref):
        o_ref[...] = jnp.dot(a_ref[...], b_ref[...], preferred_element_type=F32)

    return pl.pallas_call(
        body, name=name, grid=(m // tm,),
        in_specs=[pl.BlockSpec((tm, k), lambda i: (i, 0)), pl.BlockSpec((k, n), lambda i: (0, 0))],
        out_specs=pl.BlockSpec((tm, n), lambda i: (i, 0)),
        out_shape=jax.ShapeDtypeStruct((m, n), F32), compiler_params=_params("parallel"),
    )(a, b)


def _matmul_tn_acc(at, b, name, tk=512):
    m, k = at.shape
    n = b.shape[1]

    def body(a_ref, b_ref, o_ref):
        j = pl.program_id(0)

        @pl.when(j == 0)
        def _():
            o_ref[...] = jnp.zeros_like(o_ref)

        o_ref[...] += jnp.dot(a_ref[...], b_ref[...].astype(BF16), preferred_element_type=F32)

    return pl.pallas_call(
        body, name=name, grid=(k // tk,),
        in_specs=[pl.BlockSpec((m, tk), lambda j: (0, j)), pl.BlockSpec((tk, n), lambda j: (j, 0))],
        out_specs=pl.BlockSpec((m, n), lambda j: (0, 0)),
        out_shape=jax.ShapeDtypeStruct((m, n), F32), compiler_params=_params("arbitrary"),
    )(at, b)


def _inproj_bwd(du_a, du_b, du_g, w_a, w_b, w_g, x, dx2, g, tm=256):
    s, d = x.shape

    def body(da_ref, db_ref, dg_ref, wa_ref, wb_ref, wg_ref, x_ref, dx2_ref, g_ref, gx_ref, gg_ref):
        i = pl.program_id(0)

        @pl.when(i == 0)
        def _():
            gg_ref[...] = jnp.zeros_like(gg_ref)

        dh = _dot_nt(da_ref[...].astype(BF16), wa_ref[...])
        dh += _dot_nt(db_ref[...].astype(BF16), wb_ref[...])
        dh += _dot_nt(dg_ref[...].astype(BF16), wg_ref[...])
        xv = x_ref[...]
        r = lax.rsqrt(jnp.mean(xv * xv, axis=-1, keepdims=True) + RMS_EPS)
        xh = xv * r
        gg_ref[...] += _colsum(dh * xh)
        dxh = dh * g_ref[...]
        gx_ref[...] = dx2_ref[...] + r * (dxh - xh * jnp.mean(dxh * xh, axis=-1, keepdims=True))

    row = lambda w: pl.BlockSpec((tm, w), lambda i: (i, 0))
    full = lambda a: pl.BlockSpec(a.shape, lambda i: (0, 0))
    return pl.pallas_call(
        body, name="inproj_bwd", grid=(s // tm,),
        in_specs=[row(SEC), row(SEC), row(GATE_COLS), full(w_a), full(w_b), full(w_g), row(d), row(d), full(g)],
        out_specs=[row(d), pl.BlockSpec((1, d), lambda i: (0, 0))],
        out_shape=[jax.ShapeDtypeStruct((s, d), F32), jax.ShapeDtypeStruct((1, d), F32)],
        compiler_params=_params("arbitrary"),
    )(du_a, du_b, du_g, w_a, w_b, w_g, x, dx2, g)


def _rwkv_elementwise(ua, prev_row, first, mu, wl, w0, a0, kkw, kaw, bd):
    tm = ua.shape[0]
    rows = lax.broadcasted_iota(jnp.int32, (tm, 1), 0)
    prev = jnp.where(first, jnp.zeros_like(prev_row), prev_row)
    shifted = jnp.where(rows == 0, prev, pltpu.roll(ua, 1, 0))
    delta = shifted - ua
    us = ua + delta * mu
    r = us[:, 0:512]
    k0 = us[:, 512:1024]
    v = us[:, 1024:1536]
    lo = us[:, 1536:1664]
    gate = us[:, 1664:2176]
    lane = lax.broadcasted_iota(jnp.int32, (1, LANES), 1)
    th = jnp.tanh(lo)
    lin = jnp.where(lane < LORA, th, lo)
    ll = jnp.dot(lin.astype(BF16), wl, preferred_element_type=F32)
    sz = _sigmoid(w0 + ll[:, :512])
    e = sz * math.exp(-0.5)
    dec = jnp.exp(-e)
    a = _sigmoid(a0 + ll[:, 512:])
    kk0 = k0 * kkw
    ss = _head_sum(kk0 * kk0, bd)
    nrm = jnp.maximum(jnp.sqrt(ss), 1e-12)
    kk = kk0 / nrm
    k = k0 * (1.0 + (a - 1.0) * kaw)
    return dict(delta=delta, us=us, r=r, k0=k0, v=v, lo=lo, gate=gate, th=th, lin=lin, sz=sz, e=e, dec=dec,
                a=a, kk0=kk0, ss=ss, nrm=nrm, kk=kk, k=k)


def _rwkv_prep(u_a, mu, wl, w0, a0, kkw, kaw, tm=256):
    s = u_a.shape[0]

    def body(ua_ref, prev_ref, mu_ref, wl_ref, w0_ref, a0_ref, kkw_ref, kaw_ref,
             r_ref, w_ref, k_ref, v_ref, a_ref, b_ref, g_ref):
        i = pl.program_id(0)
        f = _rwkv_elementwise(ua_ref[...], prev_ref[7:8, :], i == 0, mu_ref[...], wl_ref[...], w0_ref[...],
                              a0_ref[...], kkw_ref[...], kaw_ref[...], _head_ones())
        r_ref[...] = f["r"]
        w_ref[...] = f["dec"]
        k_ref[...] = f["k"]
        v_ref[...] = f["v"]
        a_ref[...] = -f["kk"]
        b_ref[...] = f["kk"] * f["a"]
        g_ref[...] = f["gate"]

    vec = lambda w: pl.BlockSpec((1, w), lambda i: (0, 0))
    out = pl.BlockSpec((tm, D_HALF), lambda i: (i, 0))
    return pl.pallas_call(
        body, name="rwkv_prep", grid=(s // tm,),
        in_specs=[pl.BlockSpec((tm, SEC), lambda i: (i, 0)),
                  pl.BlockSpec((8, SEC), lambda i: (jnp.maximum(i * (tm // 8) - 1, 0), 0)),
                  vec(SEC), pl.BlockSpec((LANES, 2 * D_HALF), lambda i: (0, 0)),
                  vec(D_HALF), vec(D_HALF), vec(D_HALF), vec(D_HALF)],
        out_specs=[out] * 7,
        out_shape=[jax.ShapeDtypeStruct((s, D_HALF), F32)] * 7,
        compiler_params=_params("parallel"),
    )(u_a, u_a, mu, wl, w0, a0, kkw, kaw)


SCAN_TB = 128
N_PAIRS = 4


def _pair_sum(x, left):
    s_l = jnp.sum(jnp.where(left, x, 0.0), axis=1, keepdims=True)
    s_r = jnp.sum(jnp.where(left, 0.0, x), axis=1, keepdims=True)
    return jnp.where(left, s_l, s_r)


def _quad_consts():
    lane = lax.broadcasted_iota(jnp.int32, (HEAD, 2 * LANES), 1)
    rowi = lax.broadcasted_iota(jnp.int32, (HEAD, 2 * LANES), 0)
    diag2 = rowi == (lane & (HEAD - 1))
    r = lax.broadcasted_iota(jnp.int32, (2 * LANES, 2 * LANES), 0) >> 6
    c = lax.broadcasted_iota(jnp.int32, (2 * LANES, 2 * LANES), 1) >> 6
    return diag2, (r == c).astype(BF16)


def _rows_to_columns(x8, diag2, bd2):
    hi = x8.astype(BF16).astype(F32)
    lo = x8 - hi
    lhs = jnp.concatenate([jnp.where(diag2, piece[i:i + 1], 0.0).astype(BF16)
                           for piece in (hi, lo) for i in range(SUBLANES)], axis=0)
    res = jnp.dot(lhs, bd2, preferred_element_type=F32)
    half = SUBLANES * HEAD
    return res[:half] + res[half:]


def _diag_rows(qtile, diag2, bd2, sub_row2):
    half = SUBLANES * HEAD
    res = jnp.dot(qtile, bd2, preferred_element_type=F32)
    out = jnp.zeros((SUBLANES, 2 * LANES), F32)
    for i in range(SUBLANES):
        t = res[i * HEAD:(i + 1) * HEAD] + res[half + i * HEAD:half + (i + 1) * HEAD]
        out = jnp.where(sub_row2 == i, _colsum(jnp.where(diag2, t, 0.0)), out)
    return out


def _store_pieces(qbuf, slot, p, i, x):
    half = SUBLANES * HEAD
    inner = slice((p % 2) * LANES, (p % 2 + 1) * LANES)
    hi = x.astype(BF16)
    qbuf[slot, p // 2, i * HEAD:(i + 1) * HEAD, inner] = hi
    qbuf[slot, p // 2, half + i * HEAD:half + (i + 1) * HEAD, inner] = (x - hi.astype(F32)).astype(BF16)


def _left_half():
    return lax.broadcasted_iota(jnp.int32, (HEAD, LANES), 1) < HEAD


def _split_refs(refs, n_rows, n_out, exchange):
    n_in = len(exchange.operands) if exchange else 0
    n_ex_out = len(exchange.out_shapes) if exchange else 0
    refs = list(refs)
    rows, refs = refs[:n_rows], refs[n_rows:]
    ex_in, refs = refs[:n_in], refs[n_in:]
    outs, refs = refs[:n_out], refs[n_out:]
    ex_out, refs = refs[:n_ex_out], refs[n_ex_out:]
    scratch, sems = (refs[:-3], refs[-3:]) if exchange else (refs, None)
    moves = exchange.moves(ex_in, ex_out, sems) if exchange else None
    return rows, outs, scratch, moves


def _wkv_fwd(r, w, k, a, b, v, exchange=None):
    s = r.shape[0]
    tb = SCAN_TB
    nb = s // tb

    def body(*refs):
        (r_ref, w_ref, k_ref, a_ref, b_ref, v_ref), (y_ref, st_ref), (state, vbuf, qbuf), moves = _split_refs(
            refs, 6, 2, exchange)
        g = pl.program_id(0)
        if moves:
            moves.start(also=(g == 0))

        @pl.when(g == 0)
        def _():
            state[...] = jnp.zeros_like(state)
            qbuf[...] = jnp.zeros_like(qbuf)

        left = _left_half()
        diag2, bd2 = _quad_consts()
        sub_row2 = lax.broadcasted_iota(jnp.int32, (SUBLANES, 2 * LANES), 0)
        groups = tb // SUBLANES
        quads = [slice(g2 * 2 * LANES, (g2 + 1) * 2 * LANES) for g2 in range(2)]

        def rows_of(q):
            return pl.ds(pl.multiple_of(q * SUBLANES, SUBLANES), SUBLANES)

        def v_tiles(q, slot):
            v8 = v_ref[rows_of(q), :]
            for g2 in range(2):
                vbuf[slot, g2] = _rows_to_columns(v8[:, quads[g2]], diag2, bd2)

        def chain(q, slot):
            rows8 = rows_of(q)
            a8, w8, b8, k8, r8 = (x[rows8, :] for x in (a_ref, w_ref, b_ref, k_ref, r_ref))
            sp = [state[p] for p in range(N_PAIRS)]
            for i in range(SUBLANES):
                row = slice(i, i + 1)
                for p in range(N_PAIRS):
                    st_ref[q * SUBLANES + i, p] = sp[p]
                sa = [_pair_sum(sp[p] * a8[row, p * LANES:(p + 1) * LANES], left) for p in range(N_PAIRS)]
                for p in range(N_PAIRS):
                    lanes = slice(p * LANES, (p + 1) * LANES)
                    vt = vbuf[slot, p // 2, i * HEAD:(i + 1) * HEAD, (p % 2) * LANES:(p % 2 + 1) * LANES]
                    sp[p] = sp[p] * w8[row, lanes] + sa[p] * b8[row, lanes] + vt * k8[row, lanes]
                    _store_pieces(qbuf, slot, p, i, sp[p] * r8[row, lanes])
            for p in range(N_PAIRS):
                state[p] = sp[p]

        def y_rows(q, slot):
            for g2 in range(2):
                y_ref[rows_of(q), quads[g2]] = _diag_rows(qbuf[slot, g2], diag2, bd2, sub_row2)

        v_tiles(0, 0)

        def two_groups(j, carry):
            q0 = 2 * j
            v_tiles(q0 + 1, 1)
            chain(q0, 0)
            y_rows(jnp.maximum(q0 - 1, 0), 1)
            v_tiles(jnp.minimum(q0 + 2, groups - 1), 0)
            chain(q0 + 1, 1)
            y_rows(q0, 0)
            return carry

        lax.fori_loop(0, groups // 2, two_groups, 0)
        y_rows(groups - 1, 1)
        if moves:
            moves.wait(also=(g == nb - 1))

    rows = pl.BlockSpec((tb, D_HALF), lambda g: (g, 0))
    ex_in = exchange.operands if exchange else []
    ex_out = exchange.out_shapes if exchange else []
    res = pl.pallas_call(
        body, name="wkv_fwd", grid=(nb,),
        in_specs=[rows] * 6 + [ANY] * len(ex_in),
        out_specs=[rows, pl.BlockSpec((tb, N_PAIRS, HEAD, LANES), lambda g: (g, 0, 0, 0))] + [ANY] * len(ex_out),
        out_shape=[jax.ShapeDtypeStruct((s, D_HALF), F32),
                   jax.ShapeDtypeStruct((s, N_PAIRS, HEAD, LANES), F32)] + ex_out,
        scratch_shapes=[pltpu.VMEM((N_PAIRS, HEAD, LANES), F32),
                        pltpu.VMEM((2, 2, SUBLANES * HEAD, 2 * LANES), F32),
                        pltpu.VMEM((2, 2, 2 * SUBLANES * HEAD, 2 * LANES), BF16)]
                       + (exchange.scratch() if exchange else []),
        compiler_params=_params("arbitrary"),
    )(r, w, k, a, b, v, *ex_in)
    return res[0], res[1], list(res[2:])


def _wkv_bwd(r, w, k, a, b, v, dy, st, exchange=None):
    s = r.shape[0]
    tb = SCAN_TB
    nb = s // tb

    def body(*refs):
        ((r_ref, w_ref, k_ref, a_ref, b_ref, v_ref, dy_ref, st_ref),
         (dr_ref, dw_ref, dk_ref, dv_ref, da_ref, db_ref), (dstate, vbuf, qbuf), moves) = _split_refs(refs, 8, 6, exchange)
        g = pl.program_id(0)
        if moves:
            moves.start(also=(g == 0))

        @pl.when(g == 0)
        def _():
            dstate[...] = jnp.zeros_like(dstate)
            qbuf[...] = jnp.zeros_like(qbuf)

        left = _left_half()
        diag2, bd2 = _quad_consts()
        sub_row = lax.broadcasted_iota(jnp.int32, (SUBLANES, LANES), 0)
        sub_row2 = lax.broadcasted_iota(jnp.int32, (SUBLANES, 2 * LANES), 0)
        groups = tb // SUBLANES
        quads = [slice(g2 * 2 * LANES, (g2 + 1) * 2 * LANES) for g2 in range(2)]
        row_refs = (dr_ref, dw_ref, dk_ref, da_ref, db_ref)

        def rows_of(q):
            return pl.ds(pl.multiple_of(q * SUBLANES, SUBLANES), SUBLANES)

        def column_tiles(q, slot):
            rows8 = rows_of(q)
            for kind, ref in enumerate((v_ref, dy_ref)):
                x8 = ref[rows8, :]
                for g2 in range(2):
                    vbuf[slot, kind, g2] = _rows_to_columns(x8[:, quads[g2]], diag2, bd2)

        def chain(q, slot):
            rows8 = rows_of(q)
            a8, w8, b8, k8, r8 = (x[rows8, :] for x in (a_ref, w_ref, b_ref, k_ref, r_ref))
            dsp = [dstate[p] for p in range(N_PAIRS)]
            outs = [[jnp.zeros((SUBLANES, LANES), F32) for _ in row_refs] for _ in range(N_PAIRS)]
            for i in reversed(range(SUBLANES)):
                row = slice(i, i + 1)
                pl_ = [slice(p * LANES, (p + 1) * LANES) for p in range(N_PAIRS)]
                tile = [(p // 2, slice(i * HEAD, (i + 1) * HEAD), slice((p % 2) * LANES, (p % 2 + 1) * LANES))
                        for p in range(N_PAIRS)]
                sp = [st_ref[q * SUBLANES + i, p] for p in range(N_PAIRS)]
                dyt = [vbuf[(slot, 1) + tile[p]] for p in range(N_PAIRS)]
                ds = [dsp[p] + dyt[p] * r8[row, pl_[p]] for p in range(N_PAIRS)]
                dsa = [_pair_sum(ds[p] * b8[row, pl_[p]], left) for p in range(N_PAIRS)]
                sa = [_pair_sum(sp[p] * a8[row, pl_[p]], left) for p in range(N_PAIRS)]
                for p in range(N_PAIRS):
                    ar, wr, br, kr = (x[row, pl_[p]] for x in (a8, w8, b8, k8))
                    vt = vbuf[(slot, 0) + tile[p]]
                    dsp[p] = ds[p] * wr + dsa[p] * ar
                    sn = sp[p] * wr + sa[p] * br + vt * kr
                    new = (_colsum(sn * dyt[p]), _colsum(ds[p] * sp[p]), _colsum(ds[p] * vt),
                           _colsum(sp[p] * dsa[p]), _colsum(ds[p] * sa[p]))
                    outs[p] = [jnp.where(sub_row == i, n, o) for n, o in zip(new, outs[p])]
                    _store_pieces(qbuf, slot, p, i, ds[p] * kr)
            for p in range(N_PAIRS):
                dstate[p] = dsp[p]
                for ref, o in zip(row_refs, outs[p]):
                    ref[rows8, p * LANES:(p + 1) * LANES] = o

        def dv_rows(q, slot):
            for g2 in range(2):
                dv_ref[rows_of(q), quads[g2]] = _diag_rows(qbuf[slot, g2], diag2, bd2, sub_row2)

        column_tiles(groups - 1, 0)

        def two_groups(j, carry):
            q0 = groups - 1 - 2 * j
            column_tiles(q0 - 1, 1)
            chain(q0, 0)
            dv_rows(jnp.minimum(q0 + 1, groups - 1), 1)
            column_tiles(jnp.maximum(q0 - 2, 0), 0)
            chain(q0 - 1, 1)
            dv_rows(q0, 0)
            return carry

        lax.fori_loop(0, groups // 2, two_groups, 0)
        dv_rows(0, 1)
        if moves:
            moves.wait(also=(g == nb - 1))

    rows = pl.BlockSpec((tb, D_HALF), lambda g: (nb - 1 - g, 0))
    ex_in = exchange.operands if exchange else []
    ex_out = exchange.out_shapes if exchange else []
    res = pl.pallas_call(
        body, name="wkv_bwd", grid=(nb,),
        in_specs=[rows] * 7 + [pl.BlockSpec((tb, N_PAIRS, HEAD, LANES), lambda g: (nb - 1 - g, 0, 0, 0))]
                 + [ANY] * len(ex_in),
        out_specs=[rows] * 6 + [ANY] * len(ex_out),
        out_shape=[jax.ShapeDtypeStruct((s, D_HALF), F32)] * 6 + ex_out,
        scratch_shapes=[pltpu.VMEM((N_PAIRS, HEAD, LANES), F32),
                        pltpu.VMEM((2, 2, 2, SUBLANES * HEAD, 2 * LANES), F32),
                        pltpu.VMEM((2, 2, 2 * SUBLANES * HEAD, 2 * LANES), BF16)]
                       + (exchange.scratch() if exchange else []),
        compiler_params=_params("arbitrary"),
    )(r, w, k, a, b, v, dy, st, *ex_in)
    return list(res[:6]), list(res[6:])


def _rwkv_post_math(y, r, k, v, gate, lw, lb, rk, bd):
    mean = _head_sum(y, bd) * (1.0 / HEAD)
    yc = y - mean
    var = _head_sum(yc * yc, bd) * (1.0 / HEAD)
    rstd = lax.rsqrt(var + LNX_EPS)
    yn = yc * rstd
    rkk = _head_sum(r * k * rk, bd)
    sg = _sigmoid(gate)
    pre = yn * lw + lb + rkk * v
    return yn, rstd, rkk, sg, pre


def _rwkv_post(y, r, k, v, gate, lw, lb, rk, tm=256):
    s = y.shape[0]

    def body(y_ref, r_ref, k_ref, v_ref, g_ref, lw_ref, lb_ref, rk_ref, o_ref):
        gate_v = g_ref[...]
        _, _, _, sg, pre = _rwkv_post_math(y_ref[...], r_ref[...], k_ref[...], v_ref[...], gate_v,
                                           lw_ref[...], lb_ref[...], rk_ref[...], _head_ones())
        o_ref[...] = pre * (gate_v * sg)

    blk = pl.BlockSpec((tm, D_HALF), lambda i: (i, 0))
    vec = pl.BlockSpec((1, D_HALF), lambda i: (0, 0))
    return pl.pallas_call(
        body, name="rwkv_post", grid=(s // tm,),
        in_specs=[blk] * 5 + [vec] * 3, out_specs=blk,
        out_shape=jax.ShapeDtypeStruct((s, D_HALF), F32), compiler_params=_params("parallel"),
    )(y, r, k, v, gate, lw, lb, rk)


def _rwkv_post_bwd(dmix, y, r, k, v, gate, lw, lb, rk, tm=256):
    s = y.shape[0]

    def body(dm_ref, y_ref, r_ref, k_ref, v_ref, g_ref, lw_ref, lb_ref, rk_ref,
             dy_ref, dr_ref, dk_ref, dv_ref, dg_ref, dlw_ref, dlb_ref, drk_ref):
        i = pl.program_id(0)

        @pl.when(i == 0)
        def _():
            dlw_ref[...] = jnp.zeros_like(dlw_ref)
            dlb_ref[...] = jnp.zeros_like(dlb_ref)
            drk_ref[...] = jnp.zeros_like(drk_ref)

        bd = _head_ones()
        rv, kv, vv, gate_v, lw_v, rk_v = r_ref[...], k_ref[...], v_ref[...], g_ref[...], lw_ref[...], rk_ref[...]
        yn, rstd, rkk, sg, pre = _rwkv_post_math(y_ref[...], rv, kv, vv, gate_v, lw_v, lb_ref[...], rk_v, bd)
        dm = dm_ref[...]
        dg_ref[...] = dm * pre * (sg * (1.0 + gate_v * (1.0 - sg)))
        dpre = dm * (gate_v * sg)
        dlw_ref[...] += _colsum(dpre * yn)
        dlb_ref[...] += _colsum(dpre)
        dyn = dpre * lw_v
        m1 = _head_sum(dyn, bd) * (1.0 / HEAD)
        m2 = _head_sum(dyn * yn, bd) * (1.0 / HEAD)
        dy_ref[...] = rstd * (dyn - m1 - yn * m2)
        dv_ref[...] = dpre * rkk
        drkk = _head_sum(dpre * vv, bd)
        dr_ref[...] = drkk * kv * rk_v
        dk_ref[...] = drkk * rv * rk_v
        drk_ref[...] += _colsum(drkk * rv * kv)

    blk = pl.BlockSpec((tm, D_HALF), lambda i: (i, 0))
    vec = pl.BlockSpec((1, D_HALF), lambda i: (0, 0))
    return pl.pallas_call(
        body, name="rwkv_post_bwd", grid=(s // tm,),
        in_specs=[blk] * 6 + [vec] * 3, out_specs=[blk] * 5 + [vec] * 3,
        out_shape=[jax.ShapeDtypeStruct((s, D_HALF), F32)] * 5 + [jax.ShapeDtypeStruct((1, D_HALF), F32)] * 3,
        compiler_params=_params("arbitrary"),
    )(dmix, y, r, k, v, gate, lw, lb, rk)


def _rwkv_prep_bwd(u_a, grads, mu, wl, w0, a0, kkw, kaw, tm=256):
    s = u_a.shape[0]
    nb = s // tm

    def body(ua_ref, prev_ref, drs_ref, dws_ref, dks_ref, dvs_ref, das_ref, dbs_ref, drb_ref, dkb_ref, dvb_ref,
             dgt_ref, mu_ref, wl_ref, w0_ref, a0_ref, kkw_ref, kaw_ref,
             du_ref, dmu_ref, dwl_ref, dw0_ref, da0_ref, dkkw_ref, dkaw_ref, carry):
        i = pl.program_id(0)

        @pl.when(i == 0)
        def _():
            carry[...] = jnp.zeros_like(carry)
            for ref in (dmu_ref, dwl_ref, dw0_ref, da0_ref, dkkw_ref, dkaw_ref):
                ref[...] = jnp.zeros_like(ref)

        bd = _head_ones()
        mu_v, wl_v, kkw_v, kaw_v = mu_ref[...], wl_ref[...], kkw_ref[...], kaw_ref[...]
        f = _rwkv_elementwise(ua_ref[...], prev_ref[7:8, :], i == nb - 1, mu_v, wl_v, w0_ref[...],
                              a0_ref[...], kkw_v, kaw_v, bd)
        a, kk, k0 = f["a"], f["kk"], f["k0"]
        dk = dks_ref[...] + dkb_ref[...]
        dbs = dbs_ref[...]
        dkk = dbs * a - das_ref[...]
        da = dbs * kk + dk * k0 * kaw_v
        dk0 = dk * (1.0 + (a - 1.0) * kaw_v)
        dkaw_ref[...] += _colsum(dk * k0 * (a - 1.0))
        inv = 1.0 / f["nrm"]
        proj = _head_sum(dkk * kk, bd)
        dkk0 = jnp.where(f["ss"] > 1e-24, (dkk - kk * proj) * inv, dkk * inv)
        dk0 = dk0 + dkk0 * kkw_v
        dkkw_ref[...] += _colsum(dkk0 * k0)
        dza = da * a * (1.0 - a)
        da0_ref[...] += _colsum(dza)
        dz = -dws_ref[...] * f["dec"] * f["e"] * (1.0 - f["sz"])
        dw0_ref[...] += _colsum(dz)
        dll = jnp.concatenate([dz, dza], axis=1).astype(BF16)
        dwl_ref[...] += _dot_tn(f["lin"].astype(BF16), dll)
        dlin = _dot_nt(dll, wl_v)
        lane = lax.broadcasted_iota(jnp.int32, (1, LANES), 1)
        th = f["th"]
        dlo = jnp.where(lane < LORA, dlin * (1.0 - th * th), dlin)
        dus = jnp.concatenate([drs_ref[...] + drb_ref[...], dk0, dvs_ref[...] + dvb_ref[...], dlo, dgt_ref[...]],
                              axis=1)
        dmu_ref[...] += _colsum(dus * f["delta"])
        g1 = dus * mu_v
        rows = lax.broadcasted_iota(jnp.int32, (tm, 1), 0)
        up = jnp.where(rows == tm - 1, carry[...], pltpu.roll(g1, tm - 1, 0))
        du_ref[...] = dus - g1 + up
        carry[...] = g1[0:1, :]

    rev = lambda w: pl.BlockSpec((tm, w), lambda i: (nb - 1 - i, 0))
    vec = lambda w: pl.BlockSpec((1, w), lambda i: (0, 0))
    wl_spec = pl.BlockSpec((LANES, 2 * D_HALF), lambda i: (0, 0))
    return pl.pallas_call(
        body, name="rwkv_prep_bwd", grid=(nb,),
        in_specs=[rev(SEC), pl.BlockSpec((8, SEC), lambda i: (jnp.maximum((nb - 1 - i) * (tm // 8) - 1, 0), 0))]
                 + [rev(D_HALF)] * 10 + [vec(SEC), wl_spec] + [vec(D_HALF)] * 4,
        out_specs=[rev(SEC), vec(SEC), wl_spec] + [vec(D_HALF)] * 4,
        out_shape=[jax.ShapeDtypeStruct((s, SEC), F32), jax.ShapeDtypeStruct((1, SEC), F32),
                   jax.ShapeDtypeStruct((LANES, 2 * D_HALF), F32)] + [jax.ShapeDtypeStruct((1, D_HALF), F32)] * 4,
        scratch_shapes=[pltpu.VMEM((1, SEC), F32)],
        compiler_params=_params("arbitrary"),
    )(u_a, u_a, *grads, mu, wl, w0, a0, kkw, kaw)


def _tri(tm, lower):
    r = lax.broadcasted_iota(jnp.int32, (tm, tm), 0)
    c = lax.broadcasted_iota(jnp.int32, (tm, tm), 1)
    return ((r >= c) if lower else (r <= c)).astype(BF16)


def _head_rms(x, g, bd):
    rinv = lax.rsqrt(_head_sum(x * x, bd) * (1.0 / HEAD) + RMS_EPS)
    xh = x * rinv
    return xh, rinv, xh * g


def _fox_prep(u_b, fb, qg, kg, tm=256):
    s = u_b.shape[0]

    def body(ub_ref, fb_ref, qg_ref, kg_ref, q_ref, k_ref, v_ref, cc_ref, cr_ref, carry):
        i = pl.program_id(0)

        @pl.when(i == 0)
        def _():
            carry[...] = jnp.zeros_like(carry)

        bd = _head_ones()
        _, _, qn = _head_rms(ub_ref[:, 0:512], qg_ref[...], bd)
        _, _, kn = _head_rms(ub_ref[:, 512:1024], kg_ref[...], bd)
        q_ref[...] = (qn * ATT_SCALE).astype(BF16)
        k_ref[...] = kn.astype(BF16)
        v_ref[...] = ub_ref[:, 1024:1536].astype(BF16)
        lane = lax.broadcasted_iota(jnp.int32, (1, LANES), 1)
        logf = jnp.where(lane < N_HEADS, _log_sigmoid(ub_ref[:, 2048:2176] + fb_ref[...]), 0.0)
        cum = _exact_dot(logf, _tri(tm, True), ones_first=True) + carry[...]
        cc_ref[...] = cum
        cr_ref[...] = jnp.transpose(cum)[0:N_HEADS, :]
        carry[...] = cum[tm - 1:tm, :]

    blk = pl.BlockSpec((tm, D_HALF), lambda i: (i, 0))
    return pl.pallas_call(
        body, name="fox_prep", grid=(s // tm,),
        in_specs=[pl.BlockSpec((tm, SEC), lambda i: (i, 0)), pl.BlockSpec((1, LANES), lambda i: (0, 0)),
                  pl.BlockSpec((1, D_HALF), lambda i: (0, 0)), pl.BlockSpec((1, D_HALF), lambda i: (0, 0))],
        out_specs=[blk, blk, blk, pl.BlockSpec((tm, LANES), lambda i: (i, 0)),
                   pl.BlockSpec((N_HEADS, tm), lambda i: (0, i))],
        out_shape=[jax.ShapeDtypeStruct((s, D_HALF), BF16)] * 3
                  + [jax.ShapeDtypeStruct((s, LANES), F32), jax.ShapeDtypeStruct((N_HEADS, s), F32)],
        scratch_shapes=[pltpu.VMEM((1, LANES), F32)],
        compiler_params=_params("arbitrary"),
    )(u_b, fb, qg, kg)


ATT_T = 256


def _attn_fwd(q, k, v, cc, cr, u_b):
    s = q.shape[0]
    t = ATT_T
    nblk = s // t

    def body(q_ref, k_ref, v_ref, cc_ref, cr_ref, g_ref, o_ref, mix_ref, lse_ref, m_sc, l_sc, acc_sc):
        i = pl.program_id(0)
        j = pl.program_id(1)

        @pl.when(j == 0)
        def _():
            m_sc[...] = jnp.full_like(m_sc, NEG)
            l_sc[...] = jnp.zeros_like(l_sc)
            acc_sc[...] = jnp.zeros_like(acc_sc)

        @pl.when(j <= i)
        def _():
            row = i * t + lax.broadcasted_iota(jnp.int32, (t, t), 0)
            col = j * t + lax.broadcasted_iota(jnp.int32, (t, t), 1)
            causal = row >= col
            left = lax.broadcasted_iota(jnp.int32, (1, LANES), 1) < HEAD
            for p in range(N_PAIRS):
                lanes = slice(p * LANES, (p + 1) * LANES)
                q2, k2, v2 = q_ref[:, lanes], k_ref[:, lanes], v_ref[:, lanes]
                acc2 = acc_sc[:, lanes]
                for e in range(2):
                    h = 2 * p + e
                    msk = left if e == 0 else jnp.logical_not(left)
                    sc = _dot_nt(jnp.where(msk, q2, jnp.zeros_like(q2)), k2)
                    sc = sc + (cc_ref[:, h:h + 1] - cr_ref[h:h + 1, :])
                    sc = jnp.where(causal, sc, NEG)
                    m_prev = m_sc[h]
                    m_new = jnp.maximum(m_prev, jnp.max(sc, axis=1, keepdims=True))
                    alpha = jnp.exp(m_prev - m_new)
                    pm = jnp.exp(sc - m_new)
                    l_sc[h] = alpha * l_sc[h] + jnp.sum(pm, axis=1, keepdims=True)
                    m_sc[h] = m_new
                    pv = jnp.dot(pm.astype(BF16), v2, preferred_element_type=F32)
                    acc2 = jnp.where(msk, alpha * acc2 + pv, acc2)
                acc_sc[:, lanes] = acc2

        @pl.when(j == i)
        def _():
            lane = lax.broadcasted_iota(jnp.int32, (1, LANES), 1)
            left = lane < HEAD
            lse = jnp.zeros((t, LANES), F32)
            for p in range(N_PAIRS):
                lanes = slice(p * LANES, (p + 1) * LANES)
                inv = jnp.where(left, 1.0 / l_sc[2 * p], 1.0 / l_sc[2 * p + 1])
                o = acc_sc[:, lanes] * inv
                o_ref[:, lanes] = o
                gate = g_ref[:, lanes]
                mix_ref[:, lanes] = o * (gate * _sigmoid(gate))
                for e in range(2):
                    h = 2 * p + e
                    lse = jnp.where(lane == h, m_sc[h] + jnp.log(l_sc[h]), lse)
            lse_ref[...] = lse

    qblk = pl.BlockSpec((t, D_HALF), lambda i, j: (i, 0))
    kblk = pl.BlockSpec((t, D_HALF), lambda i, j: (jnp.minimum(i, j), 0))
    return pl.pallas_call(
        body, name="fox_attn_fwd", grid=(nblk, nblk),
        in_specs=[qblk, kblk, kblk, pl.BlockSpec((t, LANES), lambda i, j: (i, 0)),
                  pl.BlockSpec((N_HEADS, t), lambda i, j: (0, jnp.minimum(i, j))),
                  pl.BlockSpec((t, D_HALF), lambda i, j: (i, 3))],
        out_specs=[qblk, qblk, pl.BlockSpec((t, LANES), lambda i, j: (i, 0))],
        out_shape=[jax.ShapeDtypeStruct((s, D_HALF), F32), jax.ShapeDtypeStruct((s, D_HALF), F32),
                   jax.ShapeDtypeStruct((s, LANES), F32)],
        scratch_shapes=[pltpu.VMEM((N_HEADS, t, 1), F32), pltpu.VMEM((N_HEADS, t, 1), F32),
                        pltpu.VMEM((t, D_HALF), F32)],
        compiler_params=_params("parallel", "arbitrary"),
    )(q, k, v, cc, cr, u_b)


def _fox_post_bwd(dmix, o, u_b, tm=256):
    s = o.shape[0]

    def body(dm_ref, o_ref, g_ref, do_ref, dg_ref):
        gate = g_ref[...]
        sg = _sigmoid(gate)
        dm = dm_ref[...]
        do_ref[...] = (dm * (gate * sg)).astype(BF16)
        dg_ref[...] = dm * o_ref[...] * (sg * (1.0 + gate * (1.0 - sg)))

    blk = pl.BlockSpec((tm, D_HALF), lambda i: (i, 0))
    return pl.pallas_call(
        body, name="fox_post_bwd", grid=(s // tm,),
        in_specs=[blk, blk, pl.BlockSpec((tm, D_HALF), lambda i: (i, 3))], out_specs=[blk] * 2,
        out_shape=[jax.ShapeDtypeStruct((s, D_HALF), BF16), jax.ShapeDtypeStruct((s, D_HALF), F32)],
        compiler_params=_params("parallel"),
    )(dmix, o, u_b)


def _attn_probs(q2, k2, v2, do2, msk, causal, bias, lse_col):
    zero = jnp.zeros_like(q2)
    qh = jnp.where(msk, q2, zero)
    doh = jnp.where(msk, do2, zero)
    sc = jnp.where(causal, _dot_nt(qh, k2) + bias, NEG)
    pm = jnp.exp(sc - lse_col)
    dp = _dot_nt(doh, v2)
    return qh, doh, pm, dp


def _attn_bwd_rowdot(q, k, v, do, lse, cc, cr):
    s = q.shape[0]
    t = ATT_T
    nblk = s // t

    def body(q_ref, k_ref, v_ref, do_ref, lse_ref, cc_ref, cr_ref, dd_ref, acc):
        i = pl.program_id(0)
        j = pl.program_id(1)

        @pl.when(j == 0)
        def _():
            acc[...] = jnp.zeros_like(acc)

        @pl.when(j <= i)
        def _():
            row = i * t + lax.broadcasted_iota(jnp.int32, (t, t), 0)
            col = j * t + lax.broadcasted_iota(jnp.int32, (t, t), 1)
            causal = row >= col
            left = lax.broadcasted_iota(jnp.int32, (1, LANES), 1) < HEAD
            for p in range(N_PAIRS):
                lanes = slice(p * LANES, (p + 1) * LANES)
                q2, k2, v2, do2 = q_ref[:, lanes], k_ref[:, lanes], v_ref[:, lanes], do_ref[:, lanes]
                for e in range(2):
                    h = 2 * p + e
                    msk = left if e == 0 else jnp.logical_not(left)
                    bias = cc_ref[:, h:h + 1] - cr_ref[h:h + 1, :]
                    _, _, pm, dp = _attn_probs(q2, k2, v2, do2, msk, causal, bias, lse_ref[:, h:h + 1])
                    acc[h] += jnp.sum(pm * dp, axis=1, keepdims=True)

        @pl.when(j == i)
        def _():
            lane = lax.broadcasted_iota(jnp.int32, (1, LANES), 1)
            dd = jnp.zeros((t, LANES), F32)
            for h in range(N_HEADS):
                dd = jnp.where(lane == h, acc[h], dd)
            dd_ref[...] = dd

    qblk = pl.BlockSpec((t, D_HALF), lambda i, j: (i, 0))
    qcol = pl.BlockSpec((t, LANES), lambda i, j: (i, 0))
    kblk = pl.BlockSpec((t, D_HALF), lambda i, j: (jnp.minimum(i, j), 0))
    return pl.pallas_call(
        body, name="fox_attn_rowdot", grid=(nblk, nblk),
        in_specs=[qblk, kblk, kblk, qblk, qcol, qcol, pl.BlockSpec((N_HEADS, t), lambda i, j: (0, jnp.minimum(i, j)))],
        out_specs=qcol, out_shape=jax.ShapeDtypeStruct((s, LANES), F32),
        scratch_shapes=[pltpu.VMEM((N_HEADS, t, 1), F32)],
        compiler_params=_params("parallel", "arbitrary"),
    )(q, k, v, do, lse, cc, cr)


def _attn_bwd(q, k, v, do, lse, dd, cc, cr):
    s = q.shape[0]
    t = ATT_T
    nblk = s // t

    def body(q_ref, k_ref, v_ref, do_ref, lse_ref, dd_ref, cc_ref, cr_ref,
             dq_ref, dk_ref, dv_ref, dcr_ref, dk_sc, dv_sc, dcr_sc):
        j = pl.program_id(0)
        i = pl.program_id(1)

        @pl.when(jnp.logical_and(j == 0, i == 0))
        def _():
            dq_ref[...] = jnp.zeros_like(dq_ref)

        @pl.when(i == 0)
        def _():
            dk_sc[...] = jnp.zeros_like(dk_sc)
            dv_sc[...] = jnp.zeros_like(dv_sc)
            dcr_sc[...] = jnp.zeros_like(dcr_sc)

        @pl.when(i >= j)
        def _():
            row = i * t + lax.broadcasted_iota(jnp.int32, (t, t), 0)
            col = j * t + lax.broadcasted_iota(jnp.int32, (t, t), 1)
            causal = row >= col
            left = lax.broadcasted_iota(jnp.int32, (1, LANES), 1) < HEAD
            qrows = pl.ds(pl.multiple_of(i * t, t), t)
            for p in range(N_PAIRS):
                lanes = slice(p * LANES, (p + 1) * LANES)
                q2, k2, v2, do2 = q_ref[:, lanes], k_ref[:, lanes], v_ref[:, lanes], do_ref[:, lanes]
                zero = jnp.zeros_like(q2)
                dq2 = jnp.zeros((t, LANES), F32)
                dk2 = jnp.zeros((t, LANES), F32)
                dv2 = jnp.zeros((t, LANES), F32)
                for e in range(2):
                    h = 2 * p + e
                    msk = left if e == 0 else jnp.logical_not(left)
                    bias = cc_ref[:, h:h + 1] - cr_ref[h:h + 1, :]
                    qh, doh, pm, dp = _attn_probs(q2, k2, v2, do2, msk, causal, bias, lse_ref[:, h:h + 1])
                    dsc = pm * (dp - dd_ref[:, h:h + 1])
                    dsb = dsc.astype(BF16)
                    dv2 += _dot_tn(pm.astype(BF16), doh)
                    dk2 += _dot_tn(dsb, qh)
                    dq2 += jnp.dot(dsb, jnp.where(msk, k2, zero), preferred_element_type=F32)
                    dcr_sc[h:h + 1, :] += -_colsum(dsc)
                dq_ref[qrows, lanes] += dq2 * ATT_SCALE
                dk_sc[:, lanes] += dk2
                dv_sc[:, lanes] += dv2

        @pl.when(i == nblk - 1)
        def _():
            dk_ref[...] = dk_sc[...]
            dv_ref[...] = dv_sc[...]
            dcr_ref[...] = dcr_sc[...]

    qblk = pl.BlockSpec((t, D_HALF), lambda j, i: (jnp.maximum(i, j), 0))
    qcol = pl.BlockSpec((t, LANES), lambda j, i: (jnp.maximum(i, j), 0))
    kblk = pl.BlockSpec((t, D_HALF), lambda j, i: (j, 0))
    return pl.pallas_call(
        body, name="fox_attn_bwd", grid=(nblk, nblk),
        in_specs=[qblk, kblk, kblk, qblk, qcol, qcol, qcol, pl.BlockSpec((N_HEADS, t), lambda j, i: (0, j))],
        out_specs=[pl.BlockSpec((s, D_HALF), lambda j, i: (0, 0)), kblk, kblk,
                   pl.BlockSpec((N_HEADS, t), lambda j, i: (0, j))],
        out_shape=[jax.ShapeDtypeStruct((s, D_HALF), F32)] * 3 + [jax.ShapeDtypeStruct((N_HEADS, s), F32)],
        scratch_shapes=[pltpu.VMEM((t, D_HALF), F32), pltpu.VMEM((t, D_HALF), F32), pltpu.VMEM((N_HEADS, t), F32)],
        compiler_params=_params("arbitrary", "arbitrary"),
    )(q, k, v, do, lse, dd, cc, cr)


def _fox_prep_bwd(u_b, dq, dk, dv, dgate, dcum, fb, qg, kg, tm=256):
    s = u_b.shape[0]
    nb = s // tm

    def body(ub_ref, dq_ref, dk_ref, dv_ref, dg_ref, dc_ref, fb_ref, qg_ref, kg_ref,
             du_ref, dqg_ref, dkg_ref, dfb_ref, carry):
        i = pl.program_id(0)

        @pl.when(i == 0)
        def _():
            carry[...] = jnp.zeros_like(carry)
            dqg_ref[...] = jnp.zeros_like(dqg_ref)
            dkg_ref[...] = jnp.zeros_like(dkg_ref)
            dfb_ref[...] = jnp.zeros_like(dfb_ref)

        bd = _head_ones()
        for lo, g_ref, d_ref, dgain_ref in ((0, qg_ref, dq_ref, dqg_ref), (512, kg_ref, dk_ref, dkg_ref)):
            gain = g_ref[...]
            xh, rinv, _ = _head_rms(ub_ref[:, lo:lo + 512], gain, bd)
            dn = d_ref[...]
            dgain_ref[...] += _colsum(dn * xh)
            dxh = dn * gain
            du_ref[:, lo:lo + 512] = rinv * (dxh - xh * (_head_sum(dxh * xh, bd) * (1.0 / HEAD)))
        du_ref[:, 1024:1536] = dv_ref[...]
        du_ref[:, 1536:2048] = dg_ref[...]
        lane = lax.broadcasted_iota(jnp.int32, (1, LANES), 1)
        dc = dc_ref[...]
        dlogf = _exact_dot(dc, _tri(tm, False), ones_first=True) + carry[...]
        carry[...] += _colsum(dc)
        fl = ub_ref[:, 2048:2176] + fb_ref[...]
        dfl = jnp.where(lane < N_HEADS, dlogf * (1.0 - _sigmoid(fl)), 0.0)
        du_ref[:, 2048:2176] = dfl
        dfb_ref[...] += _colsum(dfl)

    rev = lambda w: pl.BlockSpec((tm, w), lambda i: (nb - 1 - i, 0))
    vec = lambda w: pl.BlockSpec((1, w), lambda i: (0, 0))
    return pl.pallas_call(
        body, name="fox_prep_bwd", grid=(nb,),
        in_specs=[rev(SEC)] + [rev(D_HALF)] * 4 + [rev(LANES), vec(LANES), vec(D_HALF), vec(D_HALF)],
        out_specs=[rev(SEC), vec(D_HALF), vec(D_HALF), vec(LANES)],
        out_shape=[jax.ShapeDtypeStruct((s, SEC), F32), jax.ShapeDtypeStruct((1, D_HALF), F32),
                   jax.ShapeDtypeStruct((1, D_HALF), F32), jax.ShapeDtypeStruct((1, LANES), F32)],
        scratch_shapes=[pltpu.VMEM((1, LANES), F32)],
        compiler_params=_params("arbitrary"),
    )(u_b, dq, dk, dv, dgate, dcum, fb, qg, kg)


def _merge(mix_a, mix_b, u_g, x, tgt, wa, wb, wo, fg, tm=256):
    s, d = x.shape

    def body(ma_ref, mb_ref, ug_ref, x_ref, t_ref, wa_ref, wb_ref, wo_ref, fg_ref,
             dx2_ref, dma_ref, dmb_ref, dug_ref, dwa_ref, dwb_ref, dwo_ref, dfg_ref, loss_ref):
        i = pl.program_id(0)

        @pl.when(i == 0)
        def _():
            for ref in (dwa_ref, dwb_ref, dwo_ref, dfg_ref, loss_ref):
                ref[...] = jnp.zeros_like(ref)

        wa_v, wb_v, wo_v, fg_v = wa_ref[...], wb_ref[...], wo_ref[...], fg_ref[...]
        ma = ma_ref[...].astype(BF16)
        mb = mb_ref[...].astype(BF16)
        ya = jnp.dot(ma, wa_v, preferred_element_type=F32)
        yb = jnp.dot(mb, wb_v, preferred_element_type=F32)
        sa = _sigmoid(ug_ref[:, 0:d])
        sb = _sigmoid(ug_ref[:, d:2 * d])
        merged = (sa * ya + sb * yb).astype(BF16)
        x2 = x_ref[...] + jnp.dot(merged, wo_v, preferred_element_type=F32)
        r2 = lax.rsqrt(jnp.mean(x2 * x2, axis=-1, keepdims=True) + RMS_EPS)
        x2h = x2 * r2
        err = x2h * fg_v - t_ref[...]
        loss_ref[...] += _colsum(err * err)
        dy = err * (1.0 / d)
        dfg_ref[...] += _colsum(dy * x2h)
        dx2h = dy * fg_v
        dx2 = r2 * (dx2h - x2h * jnp.mean(dx2h * x2h, axis=-1, keepdims=True))
        dx2_ref[...] = dx2
        dx2b = dx2.astype(BF16)
        dmerged = _dot_nt(dx2b, wo_v)
        dwo_ref[...] += _dot_tn(merged, dx2b)
        dya = dmerged * sa
        dyb = dmerged * sb
        dug_ref[:, 0:d] = dya * ya * (1.0 - sa)
        dug_ref[:, d:2 * d] = dyb * yb * (1.0 - sb)
        dyab = dya.astype(BF16)
        dybb = dyb.astype(BF16)
        dma_ref[...] = _dot_nt(dyab, wa_v)
        dmb_ref[...] = _dot_nt(dybb, wb_v)
        dwa_ref[...] += _dot_tn(ma, dyab)
        dwb_ref[...] += _dot_tn(mb, dybb)

    row = lambda w: pl.BlockSpec((tm, w), lambda i: (i, 0))
    full = lambda a: pl.BlockSpec(a.shape, lambda i: (0, 0))
    fshape = lambda a: jax.ShapeDtypeStruct(a.shape, F32)
    return pl.pallas_call(
        body, name="merge_fwd_bwd", grid=(s // tm,),
        in_specs=[row(D_HALF), row(D_HALF), row(GATE_COLS), row(d), row(d), full(wa), full(wb), full(wo), full(fg)],
        out_specs=[row(d), row(D_HALF), row(D_HALF), row(GATE_COLS), full(wa), full(wb), full(wo), full(fg), full(fg)],
        out_shape=[jax.ShapeDtypeStruct((s, d), F32), jax.ShapeDtypeStruct((s, D_HALF), F32),
                   jax.ShapeDtypeStruct((s, D_HALF), F32), jax.ShapeDtypeStruct((s, GATE_COLS), F32),
                   fshape(wa), fshape(wb), fshape(wo), fshape(fg), fshape(fg)],
        compiler_params=_params("arbitrary"),
    )(mix_a, mix_b, u_g, x, tgt, wa, wb, wo, fg)


def _lora_weight(w_up, a_up):
    z = jnp.zeros((LORA, D_HALF), w_up.dtype)
    return jnp.concatenate([jnp.concatenate([w_up, z], axis=1), jnp.concatenate([z, a_up], axis=1)], axis=0)


def _device_grads(x, tgt, p, w_a, w_up, a_up, late_weights, fwd_exchange=None, bwd_exchange=None):
    wl = _lora_weight(w_up, a_up)
    rk = p["r_k"].reshape(1, D_HALF)
    fb = jnp.pad(p["f_bias"], ((0, 0), (0, LANES - N_HEADS)))
    qg = jnp.tile(p["q_norm_g"], (1, N_HEADS))
    kg = jnp.tile(p["k_norm_g"], (1, N_HEADS))
    fg = p["final_norm_g"].reshape(1, D_MODEL)
    mixer = (p["shift_mu"], wl, p["w0"], p["a0"], p["k_k"], p["k_a"])

    h = _rmsnorm_in(x, p["norm_g"])
    u_a = _matmul_nn(h, w_a, "inproj_rwkv")
    r, dec, k, v, av, bv, gate_a = _rwkv_prep(u_a, *mixer)
    y, st, arrived = _wkv_fwd(r, dec, k, av, bv, v, fwd_exchange)
    mix_a = _rwkv_post(y, r, k, v, gate_a, p["lnx_w"], p["lnx_b"], rk)

    w_b, w_g, w_out_a, w_out_b, w_out = late_weights(arrived)
    u_b = _matmul_nn(h, w_b, "inproj_fox")
    u_g = _matmul_nn(h, w_g, "inproj_gate")
    q, kn, vb, cc, cr = _fox_prep(u_b, fb, qg, kg)
    o, mix_b, lse = _attn_fwd(q, kn, vb, cc, cr, u_b)

    dx2, dmix_a, dmix_b, du_g, dwa, dwb, dwo, dfg, loss_vec = _merge(
        mix_a, mix_b, u_g, x, tgt, w_out_a, w_out_b, w_out, fg)

    do, dgate_b = _fox_post_bwd(dmix_b, o, u_b)
    dd = _attn_bwd_rowdot(q, kn, vb, do, lse, cc, cr)
    dq, dk_att, dv_att, dcr = _attn_bwd(q, kn, vb, do, lse, dd, cc, cr)
    dcum = jnp.pad(dcr.T, ((0, 0), (0, LANES - N_HEADS)))
    du_b, dqg, dkg, dfb = _fox_prep_bwd(u_b, dq, dk_att, dv_att, dgate_b, dcum, fb, qg, kg)
    h_t = h.T
    dw_b = _matmul_tn_acc(h_t, du_b, "dw_fox")
    dw_g = _matmul_tn_acc(h_t, du_g, "dw_gate")

    dy, dr_b, dk_b, dv_b, dgate_a, dlw, dlb, drk = _rwkv_post_bwd(
        dmix_a, y, r, k, v, gate_a, p["lnx_w"], p["lnx_b"], rk)
    scan_grads, sent = _wkv_bwd(r, dec, k, av, bv, v, dy, st,
                                bwd_exchange(dw_b, dw_g, dwa, dwb, dwo) if bwd_exchange else None)
    du_a, dmu, dwl, dw0, da0, dkkw, dkaw = _rwkv_prep_bwd(u_a, (*scan_grads, dr_b, dk_b, dv_b, dgate_a), *mixer)
    dw_a = _matmul_tn_acc(h_t, du_a, "dw_rwkv")
    grad_x, dnorm_g = _inproj_bwd(du_a, du_b, du_g, w_a, w_b, w_g, x, dx2, p["norm_g"])

    grads = dict(
        norm_g=dnorm_g, w_in=(dw_a, dw_b, dw_g), shift_mu=dmu,
        w_lora_up=dwl[:LORA, :D_HALF], w0=dw0, a_lora_up=dwl[LORA:, D_HALF:], a0=da0, k_k=dkkw, k_a=dkaw,
        r_k=drk.reshape(1, N_HEADS, HEAD), lnx_w=dlw, lnx_b=dlb, f_bias=dfb[:, :N_HEADS],
        q_norm_g=dqg.reshape(N_HEADS, HEAD).sum(axis=0, keepdims=True),
        k_norm_g=dkg.reshape(N_HEADS, HEAD).sum(axis=0, keepdims=True),
        w_out_a=dwa, w_out_b=dwb, w_out=dwo, final_norm_g=dfg.reshape(D_MODEL))
    return loss_vec, grad_x, grads, sent


CHIP_FLIPS = ((1, 0), (0, 1), (1, 1))
ANY = pl.BlockSpec(memory_space=pl.ANY)


def _position():
    return lax.axis_index("x"), lax.axis_index("y"), lax.axis_index("c")


def _flip(v, f):
    return 1 - v if f else v


def _both(a, b):
    if a is None:
        return b
    return a if b is None else jnp.logical_and(a, b)


def _when(cond, fn):
    if cond is None:
        fn()
    else:
        pl.when(cond)(fn)


class _Moves:
    def __init__(self, send_sems, recv_sems, local_sems):
        self.send_sems, self.recv_sems, self.local_sems = send_sems, recv_sems, local_sems
        self.remote, self.local = [], []

    def send(self, src, dst, peer, landing, send_if=None, recv_if=None):
        k = len(self.remote)
        sems = dict(send_sem=self.send_sems.at[k], recv_sem=self.recv_sems.at[k], device_id=peer, device_id_type=MESH)
        out = pltpu.make_async_remote_copy(src_ref=src, dst_ref=dst, **sems)
        arrival = pltpu.make_async_remote_copy(src_ref=src, dst_ref=landing, **sems)
        self.remote.append((out, arrival, send_if, recv_if))

    def copy(self, src, dst, cond=None):
        cp = pltpu.make_async_copy(src, dst, self.local_sems.at[len(self.local)])
        self.local.append((cp, cond))

    def start(self, also=None):
        for cp, cond in self.local:
            _when(_both(also, cond), cp.start)
        for out, _, send_if, _ in self.remote:
            _when(_both(also, send_if), out.start)

    def wait(self, also=None):
        for _, arrival, _, recv_if in self.remote:
            _when(_both(also, recv_if), arrival.wait_recv)
        for out, _, send_if, _ in self.remote:
            _when(_both(also, send_if), out.wait_send)
        for cp, cond in self.local:
            _when(_both(also, cond), cp.wait)


class _Exchange:
    def __init__(self, operands, out_shapes, n_remote, n_local, build):
        self.operands, self.out_shapes = list(operands), list(out_shapes)
        self.n_remote, self.n_local, self.build = n_remote, n_local, build

    def scratch(self):
        return [pltpu.SemaphoreType.DMA((self.n_remote,)), pltpu.SemaphoreType.DMA((self.n_remote,)),
                pltpu.SemaphoreType.DMA((max(self.n_local, 1),))]

    def moves(self, in_refs, out_refs, sems):
        mv = _Moves(*sems)
        self.build(mv, in_refs, out_refs)
        return mv

    def run_alone(self, name):
        n_in, n_out = len(self.operands), len(self.out_shapes)

        def body(*refs):
            mv = self.moves(refs[:n_in], refs[n_in:n_in + n_out], refs[n_in + n_out:])
            mv.start()
            mv.wait()

        return pl.pallas_call(
            body, name=name, in_specs=[ANY] * n_in, out_specs=[ANY] * n_out, out_shape=self.out_shapes,
            scratch_shapes=self.scratch(), compiler_params=pltpu.CompilerParams(has_side_effects=True),
        )(*self.operands)


def _gather_exchange(half, w_in_shard, others):
    n = len(others)

    def build(mv, ins, outs):
        x, y, c = _position()
        me = 2 * x + y
        mv.copy(ins[0], outs[0].at[y], cond=(x == half))
        for t in range(n):
            mv.copy(ins[1 + t], outs[1 + t].at[me])
        for fx, fy in CHIP_FLIPS:
            px, py = _flip(x, fx), _flip(y, fy)
            peer = (px, py, c)
            mv.send(ins[0], outs[0].at[y], peer, landing=outs[0].at[py], send_if=(x == half), recv_if=(px == half))
            for t in range(n):
                mv.send(ins[1 + t], outs[1 + t].at[me], peer, landing=outs[1 + t].at[2 * px + py])

    shapes = [jax.ShapeDtypeStruct((2,) + w_in_shard.shape, w_in_shard.dtype)]
    shapes += [jax.ShapeDtypeStruct((N_CHIPS,) + a.shape, a.dtype) for a in others]
    return _Exchange([w_in_shard] + list(others), shapes, len(CHIP_FLIPS) * (1 + n), 1 + n, build)


def _scatter_exchange(half, w_in_blocks, stacks):
    n = len(stacks)

    def build(mv, ins, outs):
        x, y, c = _position()
        for f, (fx, fy) in enumerate(CHIP_FLIPS):
            px, py = _flip(x, fx), _flip(y, fy)
            peer = (px, py, c)
            mv.send(ins[0].at[py], outs[0].at[f], peer, landing=outs[0].at[f], send_if=(px == half), recv_if=(x == half))
            for t in range(n):
                mv.send(ins[1 + t].at[2 * px + py], outs[1 + t].at[f], peer, landing=outs[1 + t].at[f])

    shapes = [jax.ShapeDtypeStruct((len(CHIP_FLIPS),) + a.shape[1:], a.dtype) for a in [w_in_blocks] + list(stacks)]
    return _Exchange([w_in_blocks] + list(stacks), shapes, len(CHIP_FLIPS) * (1 + n), 0, build)


def _swap_sibling(tensors):
    n = len(tensors)

    def body(*refs):
        ins, outs = refs[:n], refs[n:2 * n]
        send_sems, recv_sems = refs[2 * n:]
        x, y, c = _position()
        copies = [pltpu.make_async_remote_copy(
            src_ref=ins[t], dst_ref=outs[t], send_sem=send_sems.at[t], recv_sem=recv_sems.at[t],
            device_id=(x, y, 1 - c), device_id_type=MESH) for t in range(n)]
        for cp in copies:
            cp.start()
        for cp in copies:
            cp.wait_recv()
        for cp in copies:
            cp.wait_send()

    return pl.pallas_call(
        body, name="swap_sibling", in_specs=[ANY] * n, out_specs=[ANY] * n,
        out_shape=[jax.ShapeDtypeStruct(a.shape, a.dtype) for a in tensors],
        scratch_shapes=[pltpu.SemaphoreType.DMA((n,)), pltpu.SemaphoreType.DMA((n,))],
        compiler_params=pltpu.CompilerParams(has_side_effects=True),
    )(*tensors)


def _allreduce_small(slab):
    stages = 3

    def body(x_ref, o_ref, buf, send_sems, recv_sems):
        x, y, c = _position()
        peers = ((1 - x, y, c), (x, 1 - y, c), (x, y, 1 - c))
        o_ref[...] = x_ref[...]
        for k, peer in enumerate(peers):
            cp = pltpu.make_async_remote_copy(src_ref=o_ref, dst_ref=buf.at[k], send_sem=send_sems.at[k],
                                              recv_sem=recv_sems.at[k], device_id=peer, device_id_type=MESH)
            cp.start()
            cp.wait()
            o_ref[...] = o_ref[...] + buf[k]

    return pl.pallas_call(
        body, name="allreduce_small",
        in_specs=[pl.BlockSpec(memory_space=pltpu.VMEM)], out_specs=pl.BlockSpec(memory_space=pltpu.VMEM),
        out_shape=jax.ShapeDtypeStruct(slab.shape, slab.dtype),
        scratch_shapes=[pltpu.VMEM((stages,) + slab.shape, slab.dtype),
                        pltpu.SemaphoreType.DMA((stages,)), pltpu.SemaphoreType.DMA((stages,))],
        compiler_params=pltpu.CompilerParams(has_side_effects=True),
    )(slab)


def _row_tile(r):
    return min(r, 256)


def _sum4(stack, recv, me):
    _, r, c = stack.shape
    tr = _row_tile(r)

    def body(me_ref, own_ref, recv_ref, o_ref):
        o_ref[...] = (((own_ref[...] + recv_ref[0].astype(F32)) + recv_ref[1].astype(F32))
                      + recv_ref[2].astype(F32))

    return pl.pallas_call(
        body, name="sum_partials",
        grid_spec=pltpu.PrefetchScalarGridSpec(
            num_scalar_prefetch=1, grid=(r // tr,),
            in_specs=[pl.BlockSpec((None, tr, c), lambda i, me_ref: (me_ref[0], i, 0)),
                      pl.BlockSpec((len(CHIP_FLIPS), tr, c), lambda i, me_ref: (0, i, 0))],
            out_specs=pl.BlockSpec((tr, c), lambda i, me_ref: (i, 0))),
        out_shape=jax.ShapeDtypeStruct((r, c), F32), compiler_params=_params("parallel"),
    )(me, stack, recv)


def _adamw_math(w, g, m, v):
    m = ADAM_B1 * m + (1.0 - ADAM_B1) * g
    v = ADAM_B2 * v + (1.0 - ADAM_B2) * (g * g)
    m_hat = m / (1.0 - ADAM_B1 ** ADAM_STEP)
    v_hat = v / (1.0 - ADAM_B2 ** ADAM_STEP)
    delta = -ADAM_LR * (m_hat / (jnp.sqrt(v_hat) + ADAM_EPS) + ADAM_WD * w)
    return delta, m, v


def _adamw(w, m, v, g_parts, name):
    r, c = w.shape
    tr = _row_tile(r)
    n = len(g_parts)

    def body(*refs):
        w_ref, m_ref, v_ref = refs[:3]
        g_refs = refs[3:3 + n]
        g_out, d_out, m_out, v_out = refs[3 + n:]
        g = g_refs[0][...]
        for ref in g_refs[1:]:
            g = g + ref[...]
        g_out[...] = g
        d_out[...], m_out[...], v_out[...] = _adamw_math(w_ref[...], g, m_ref[...], v_ref[...])

    blk = pl.BlockSpec((tr, c), lambda i: (i, 0))
    return pl.pallas_call(
        body, name=name, grid=(r // tr,), in_specs=[blk] * (3 + n), out_specs=[blk] * 4,
        out_shape=[jax.ShapeDtypeStruct((r, c), F32)] * 4, compiler_params=_params("parallel"),
    )(w, m, v, *g_parts)


SHARDED = ("w_in", "w_lora_up", "a_lora_up", "w_out_a", "w_out_b", "w_out")
ROW_SHARDED = ("w_out",)
SMALL = ("norm_g", "shift_mu", "w0", "a0", "k_k", "k_a", "r_k", "lnx_w", "lnx_b", "f_bias", "q_norm_g", "k_norm_g",
         "final_norm_g")
WEIGHTS = ("norm_g", "w_in", "shift_mu", "w_lora_up", "w0", "a_lora_up", "a0", "k_k", "k_a", "r_k", "lnx_w", "lnx_b",
           "f_bias", "q_norm_g", "k_norm_g", "w_out_a", "w_out_b", "w_out", "final_norm_g")
SLAB_ROWS = 16
SLAB_COLS = SEC


def _to_slab(named, extra=None):
    rows = [jnp.pad(named[n].reshape(1, -1), ((0, 0), (0, SLAB_COLS - named[n].size))) for n in SMALL]
    if extra is not None:
        rows.append(jnp.pad(extra.reshape(1, -1), ((0, 0), (0, SLAB_COLS - extra.size))))
    rows.append(jnp.zeros((SLAB_ROWS - len(rows), SLAB_COLS), F32))
    return jnp.concatenate(rows, axis=0)


def _from_slab(slab, shapes):
    return {n: slab[i, :math.prod(shapes[n])].reshape(shapes[n]) for i, n in enumerate(SMALL)}


def _by_chip(g, name):
    if name in ROW_SHARDED:
        return g.reshape(N_CHIPS, g.shape[0] // N_CHIPS, g.shape[1])
    r, c = g.shape
    return g.reshape(r, N_CHIPS, c // N_CHIPS).transpose(1, 0, 2)


def _from_chips(stack, name):
    if name in ROW_SHARDED:
        return stack.reshape(-1, stack.shape[2])
    _, r, c = stack.shape
    return stack.transpose(1, 0, 2).reshape(r, N_CHIPS * c)


def kernel(x, norm_g, w_in, shift_mu, w_lora_up, w0, a_lora_up, a0, k_k, k_a, r_k, lnx_w, lnx_b, f_bias, q_norm_g, k_norm_g, w_out_a, w_out_b, w_out, final_norm_g, loss_target, m_norm_g, m_w_in, m_shift_mu, m_w_lora_up, m_w0, m_a_lora_up, m_a0, m_k_k, m_k_a, m_r_k, m_lnx_w, m_lnx_b, m_f_bias, m_q_norm_g, m_k_norm_g, m_w_out_a, m_w_out_b, m_w_out, m_final_norm_g, v_norm_g, v_w_in, v_shift_mu, v_w_lora_up, v_w0, v_a_lora_up, v_a0, v_k_k, v_k_a, v_r_k, v_lnx_w, v_lnx_b, v_f_bias, v_q_norm_g, v_k_norm_g, v_w_out_a, v_w_out_b, v_w_out, v_final_norm_g):
    w = dict(norm_g=norm_g, w_in=w_in, shift_mu=shift_mu, w_lora_up=w_lora_up, w0=w0, a_lora_up=a_lora_up, a0=a0,
             k_k=k_k, k_a=k_a, r_k=r_k, lnx_w=lnx_w, lnx_b=lnx_b, f_bias=f_bias, q_norm_g=q_norm_g,
             k_norm_g=k_norm_g, w_out_a=w_out_a, w_out_b=w_out_b, w_out=w_out, final_norm_g=final_norm_g)
    m = dict(norm_g=m_norm_g, w_in=m_w_in, shift_mu=m_shift_mu, w_lora_up=m_w_lora_up, w0=m_w0,
             a_lora_up=m_a_lora_up, a0=m_a0, k_k=m_k_k, k_a=m_k_a, r_k=m_r_k, lnx_w=m_lnx_w, lnx_b=m_lnx_b,
             f_bias=m_f_bias, q_norm_g=m_q_norm_g, k_norm_g=m_k_norm_g, w_out_a=m_w_out_a, w_out_b=m_w_out_b,
             w_out=m_w_out, final_norm_g=m_final_norm_g)
    v = dict(norm_g=v_norm_g, w_in=v_w_in, shift_mu=v_shift_mu, w_lora_up=v_w_lora_up, w0=v_w0,
             a_lora_up=v_a_lora_up, a0=v_a0, k_k=v_k_k, k_a=v_k_a, r_k=v_r_k, lnx_w=v_lnx_w, lnx_b=v_lnx_b,
             f_bias=v_f_bias, q_norm_g=v_q_norm_g, k_norm_g=v_k_norm_g, w_out_a=v_w_out_a, w_out_b=v_w_out_b,
             w_out=v_w_out, final_norm_g=v_final_norm_g)
    shapes = {n: w[n].shape for n in WEIGHTS}

    shard = {n: w[n][0].astype(BF16) for n in SHARDED}
    late = ("w_out_a", "w_out_b", "w_out")
    w01, up_stack, aup_stack = _gather_exchange(0, shard["w_in"], [shard["w_lora_up"], shard["a_lora_up"]]).run_alone(
        "gather_early")
    w_a = jnp.concatenate([w01[0], w01[1][:, :A_TAIL]], axis=1)

    def late_weights(arrived):
        w23 = arrived[0]
        w_b = jnp.concatenate([w01[1][:, A_TAIL:], w23[0][:, :B_TAIL], jnp.zeros((D_MODEL, SEC - FOX_REAL), BF16)], axis=1)
        w_g = jnp.concatenate([w23[0][:, B_TAIL:], w23[1]], axis=1)
        return (w_b, w_g, *[_from_chips(s, n) for n, s in zip(late, arrived[1:])])

    kept = {}

    def bwd_exchange(dw_b, dw_g, dwa, dwb, dwo):
        kept["w_in_1"] = jnp.stack([jnp.concatenate([dw_b[:, B_HEAD:FOX_REAL], dw_g[:, :G_HEAD]], axis=1), dw_g[:, G_HEAD:]])
        kept.update({n: _by_chip(g, n) for n, g in zip(late, (dwa, dwb, dwo))})
        return _scatter_exchange(1, kept["w_in_1"].astype(BF16), [kept[n].astype(BF16) for n in late])

    small = {n: w[n] for n in SMALL}
    loss_vec, grad_x, grads, sent = _device_grads(
        x[0], loss_target[0], small, w_a, _from_chips(up_stack, "w_lora_up"), _from_chips(aup_stack, "a_lora_up"),
        late_weights, _gather_exchange(1, shard["w_in"], [shard[n] for n in late]), bwd_exchange)

    total = _allreduce_small(_to_slab(grads, extra=loss_vec))
    loss = (0.5 / D_MODEL) * jnp.sum(total[len(SMALL)])
    slab_g, slab_d, slab_m, slab_v = _adamw(_to_slab(w), _to_slab(m), _to_slab(v), [total], "adamw_small")
    out_g, out_d, out_m, out_v = (_from_slab(s, shapes) for s in (total, slab_d, slab_m, slab_v))
    del slab_g

    dw_a, dw_b, _ = grads["w_in"]
    w_in_0 = jnp.stack([dw_a[:, :SHARD_COLS], jnp.concatenate([dw_a[:, SHARD_COLS:], dw_b[:, :B_HEAD]], axis=1)])
    loras = ("w_lora_up", "a_lora_up")
    kept.update({n: _by_chip(grads[n], n) for n in loras})
    sent_late = _scatter_exchange(0, w_in_0.astype(BF16), [kept[n].astype(BF16) for n in loras]).run_alone("scatter_late")
    xpos, ypos, _ = _position()
    y_index = ypos.astype(jnp.int32).reshape(1)
    me = (2 * xpos + ypos).astype(jnp.int32).reshape(1)
    core_sum = {"w_in": lax.cond(xpos == 1, lambda: _sum4(kept["w_in_1"], sent[0], y_index),
                                 lambda: _sum4(w_in_0, sent_late[0], y_index))}
    core_sum.update({n: _sum4(kept[n], r, me) for n, r in zip(late, sent[1:])})
    core_sum.update({n: _sum4(kept[n], r, me) for n, r in zip(loras, sent_late[1:])})
    sibling_sums = _swap_sibling([core_sum[n] for n in SHARDED])
    for n, theirs in zip(SHARDED, sibling_sums):
        g, d, m2, v2 = _adamw(w[n][0], m[n][0], v[n][0], [core_sum[n], theirs], "adamw_" + n)
        out_g[n], out_d[n], out_m[n], out_v[n] = (a.reshape(shapes[n]) for a in (g, d, m2, v2))

    return (loss, grad_x.reshape(x.shape), *[out_g[n] for n in WEIGHTS], *[out_d[n] for n in WEIGHTS],
            *[out_m[n] for n in WEIGHTS], *[out_v[n] for n in WEIGHTS])
```

```python
import functools
import math

import jax
import jax.numpy as jnp
from jax import lax
from jax.experimental import pallas as pl
from jax.experimental.pallas import tpu as pltpu

F32 = jnp.float32
BF16 = jnp.bfloat16

D_MODEL = 1024
D_HALF = 512
HEAD = 64
N_HEADS = 8
LORA = 64
RWKV_COLS = 2176
FOX_REAL = 2056
SEC = 2176
GATE_COLS = 2048
IN_COLS = 6280
N_CHIPS = 4
SHARD_COLS = IN_COLS // N_CHIPS
A_TAIL = RWKV_COLS - SHARD_COLS
B_HEAD = SHARD_COLS - A_TAIL
B_TAIL = FOX_REAL - B_HEAD
G_HEAD = SHARD_COLS - B_TAIL
RMS_EPS = 1e-6
LNX_EPS = 64e-5
ATT_SCALE = HEAD ** -0.5
NEG = -1e30

ADAM_LR = 0.001
ADAM_B1 = 0.9
ADAM_B2 = 0.999
ADAM_EPS = 1e-08
ADAM_WD = 0.01
ADAM_STEP = 10

LANES = 128
SUBLANES = 8
VMEM_LIMIT = 56 * 1024 * 1024
MESH = pl.DeviceIdType.MESH


def _params(*sem):
    return pltpu.CompilerParams(dimension_semantics=sem if sem else None, vmem_limit_bytes=VMEM_LIMIT)


def _sigmoid(x):
    return 1.0 / (1.0 + jnp.exp(-x))


def _log_sigmoid(x):
    return jnp.minimum(x, 0.0) - jnp.log(1.0 + jnp.exp(-jnp.abs(x)))


def _head_ones():
    r = lax.broadcasted_iota(jnp.int32, (LANES, LANES), 0) >> 6
    c = lax.broadcasted_iota(jnp.int32, (LANES, LANES), 1) >> 6
    return (r == c).astype(BF16)


def _split3(x):
    hi = x.astype(BF16)
    r1 = x - hi.astype(F32)
    mid = r1.astype(BF16)
    lo = (r1 - mid.astype(F32)).astype(BF16)
    return hi, mid, lo


def _exact_dot(x, ones_bf16, ones_first=False):
    out = None
    for piece in _split3(x):
        if ones_first:
            t = jnp.dot(ones_bf16, piece, preferred_element_type=F32)
        else:
            t = jnp.dot(piece, ones_bf16, preferred_element_type=F32)
        out = t if out is None else out + t
    return out


def _head_sum(x, bd):
    n = x.shape[1] // LANES
    parts = [_exact_dot(x[:, i * LANES:(i + 1) * LANES], bd) for i in range(n)]
    return parts[0] if n == 1 else jnp.concatenate(parts, axis=1)


def _dot_nt(a, b):
    return lax.dot_general(a, b, (((1,), (1,)), ((), ())), preferred_element_type=F32)


def _dot_tn(a, b):
    return lax.dot_general(a, b, (((0,), (0,)), ((), ())), preferred_element_type=F32)


def _colsum(x):
    return jnp.sum(x, axis=0, keepdims=True)


def _rmsnorm_in(x, g, tm=512):
    s, d = x.shape

    def body(x_ref, g_ref, h_ref):
        xv = x_ref[...]
        r = lax.rsqrt(jnp.mean(xv * xv, axis=-1, keepdims=True) + RMS_EPS)
        h_ref[...] = (xv * r * g_ref[...]).astype(BF16)

    return pl.pallas_call(
        body, name="rmsnorm_in", grid=(s // tm,),
        in_specs=[pl.BlockSpec((tm, d), lambda i: (i, 0)), pl.BlockSpec((1, d), lambda i: (0, 0))],
        out_specs=pl.BlockSpec((tm, d), lambda i: (i, 0)),
        out_shape=jax.ShapeDtypeStruct((s, d), BF16), compiler_params=_params("parallel"),
    )(x, g)


def _matmul_nn(a, b, name, tm=512):
    m, k = a.shape
    n = b.shape[1]

    def body(a_ref, b_ref, o_ref):
        o_ref[...] = jnp.dot(a_ref[...], b_ref[...], preferred_element_type=F32)

    return pl.pallas_call(
        body, name=name, grid=(m // tm,),
        in_specs=[pl.BlockSpec((tm, k), lambda i: (i, 0)), pl.BlockSpec((k, n), lambda i: (0, 0))],
        out_specs=pl.BlockSpec((tm, n), lambda i: (i, 0)),
        out_shape=jax.ShapeDtypeStruct((m, n), F32), compiler_params=_params("parallel"),
    )(a, b)


def _matmul_tn_acc(at, b, name, tk=512):
    m, k = at.shape
    n = b.shape[1]

    def body(a_ref, b_ref, o_ref):
        j = pl.program_id(0)

        @pl.when(j == 0)
        def _():
            o_ref[...] = jnp.zeros_like(o_ref)

        o_ref[...] += jnp.dot(a_ref[...], b_ref[...].astype(BF16), preferred_element_type=F32)

    return pl.pallas_call(
        body, name=name, grid=(k // tk,),
        in_specs=[pl.BlockSpec((m, tk), lambda j: (0, j)), pl.BlockSpec((tk, n), lambda j: (j, 0))],
        out_specs=pl.BlockSpec((m, n), lambda j: (0, 0)),
        out_shape=jax.ShapeDtypeStruct((m, n), F32), compiler_params=_params("arbitrary"),
    )(at, b)


def _inproj_bwd(du_a, du_b, du_g, w_a, w_b, w_g, x, dx2, g, tm=256):
    s, d = x.shape

    def body(da_ref, db_ref, dg_ref, wa_ref, wb_ref, wg_ref, x_ref, dx2_ref, g_ref, gx_ref, gg_ref):
        i = pl.program_id(0)

        @pl.when(i == 0)
        def _():
            gg_ref[...] = jnp.zeros_like(gg_ref)

        dh = _dot_nt(da_ref[...].astype(BF16), wa_ref[...])
        dh += _dot_nt(db_ref[...].astype(BF16), wb_ref[...])
        dh += _dot_nt(dg_ref[...].astype(BF16), wg_ref[...])
        xv = x_ref[...]
        r = lax.rsqrt(jnp.mean(xv * xv, axis=-1, keepdims=True) + RMS_EPS)
        xh = xv * r
        gg_ref[...] += _colsum(dh * xh)
        dxh = dh * g_ref[...]
        gx_ref[...] = dx2_ref[...] + r * (dxh - xh * jnp.mean(dxh * xh, axis=-1, keepdims=True))

    row = lambda w: pl.BlockSpec((tm, w), lambda i: (i, 0))
    full = lambda a: pl.BlockSpec(a.shape, lambda i: (0, 0))
    return pl.pallas_call(
        body, name="inproj_bwd", grid=(s // tm,),
        in_specs=[row(SEC), row(SEC), row(GATE_COLS), full(w_a), full(w_b), full(w_g), row(d), row(d), full(g)],
        out_specs=[row(d), pl.BlockSpec((1, d), lambda i: (0, 0))],
        out_shape=[jax.ShapeDtypeStruct((s, d), F32), jax.ShapeDtypeStruct((1, d), F32)],
        compiler_params=_params("arbitrary"),
    )(du_a, du_b, du_g, w_a, w_b, w_g, x, dx2, g)


def _rwkv_elementwise(ua, prev_row, first, mu, wl, w0, a0, kkw, kaw, bd):
    tm = ua.shape[0]
    rows = lax.broadcasted_iota(jnp.int32, (tm, 1), 0)
    prev = jnp.where(first, jnp.zeros_like(prev_row), prev_row)
    shifted = jnp.where(rows == 0, prev, pltpu.roll(ua, 1, 0))
    delta = shifted - ua
    us = ua + delta * mu
    r = us[:, 0:512]
    k0 = us[:, 512:1024]
    v = us[:, 1024:1536]
    lo = us[:, 1536:1664]
    gate = us[:, 1664:2176]
    lane = lax.broadcasted_iota(jnp.int32, (1, LANES), 1)
    th = jnp.tanh(lo)
    lin = jnp.where(lane < LORA, th, lo)
    ll = jnp.dot(lin.astype(BF16), wl, preferred_element_type=F32)
    sz = _sigmoid(w0 + ll[:, :512])
    e = sz * math.exp(-0.5)
    dec = jnp.exp(-e)
    a = _sigmoid(a0 + ll[:, 512:])
    kk0 = k0 * kkw
    ss = _head_sum(kk0 * kk0, bd)
    nrm = jnp.maximum(jnp.sqrt(ss), 1e-12)
    kk = kk0 / nrm
    k = k0 * (1.0 + (a - 1.0) * kaw)
    return dict(delta=delta, us=us, r=r, k0=k0, v=v, lo=lo, gate=gate, th=th, lin=lin, sz=sz, e=e, dec=dec,
                a=a, kk0=kk0, ss=ss, nrm=nrm, kk=kk, k=k)


def _rwkv_prep(u_a, mu, wl, w0, a0, kkw, kaw, tm=256):
    s = u_a.shape[0]

    def body(ua_ref, prev_ref, mu_ref, wl_ref, w0_ref, a0_ref, kkw_ref, kaw_ref,
             r_ref, w_ref, k_ref, v_ref, a_ref, b_ref, g_ref):
        i = pl.program_id(0)
        f = _rwkv_elementwise(ua_ref[...], prev_ref[7:8, :], i == 0, mu_ref[...], wl_ref[...], w0_ref[...],
                              a0_ref[...], kkw_ref[...], kaw_ref[...], _head_ones())
        r_ref[...] = f["r"]
        w_ref[...] = f["dec"]
        k_ref[...] = f["k"]
        v_ref[...] = f["v"]
        a_ref[...] = -f["kk"]
        b_ref[...] = f["kk"] * f["a"]
        g_ref[...] = f["gate"]

    vec = lambda w: pl.BlockSpec((1, w), lambda i: (0, 0))
    out = pl.BlockSpec((tm, D_HALF), lambda i: (i, 0))
    return pl.pallas_call(
        body, name="rwkv_prep", grid=(s // tm,),
        in_specs=[pl.BlockSpec((tm, SEC), lambda i: (i, 0)),
                  pl.BlockSpec((8, SEC), lambda i: (jnp.maximum(i * (tm // 8) - 1, 0), 0)),
                  vec(SEC), pl.BlockSpec((LANES, 2 * D_HALF), lambda i: (0, 0)),
                  vec(D_HALF), vec(D_HALF), vec(D_HALF), vec(D_HALF)],
        out_specs=[out] * 7,
        out_shape=[jax.ShapeDtypeStruct((s, D_HALF), F32)] * 7,
        compiler_params=_params("parallel"),
    )(u_a, u_a, mu, wl, w0, a0, kkw, kaw)


SCAN_TB = 128
N_PAIRS = 4


def _pair_sum(x, left):
    s_l = jnp.sum(jnp.where(left, x, 0.0), axis=1, keepdims=True)
    s_r = jnp.sum(jnp.where(left, 0.0, x), axis=1, keepdims=True)
    return jnp.where(left, s_l, s_r)


def _quad_consts():
    lane = lax.broadcasted_iota(jnp.int32, (HEAD, 2 * LANES), 1)
    rowi = lax.broadcasted_iota(jnp.int32, (HEAD, 2 * LANES), 0)
    diag2 = rowi == (lane & (HEAD - 1))
    r = lax.broadcasted_iota(jnp.int32, (2 * LANES, 2 * LANES), 0) >> 6
    c = lax.broadcasted_iota(jnp.int32, (2 * LANES, 2 * LANES), 1) >> 6
    return diag2, (r == c).astype(BF16)


def _rows_to_columns(x8, diag2, bd2):
    hi = x8.astype(BF16).astype(F32)
    lo = x8 - hi
    lhs = jnp.concatenate([jnp.where(diag2, piece[i:i + 1], 0.0).astype(BF16)
                           for piece in (hi, lo) for i in range(SUBLANES)], axis=0)
    res = jnp.dot(lhs, bd2, preferred_element_type=F32)
    half = SUBLANES * HEAD
    return res[:half] + res[half:]


def _diag_rows(qtile, diag2, bd2, sub_row2):
    half = SUBLANES * HEAD
    res = jnp.dot(qtile, bd2, preferred_element_type=F32)
    out = jnp.zeros((SUBLANES, 2 * LANES), F32)
    for i in range(SUBLANES):
        t = res[i * HEAD:(i + 1) * HEAD] + res[half + i * HEAD:half + (i + 1) * HEAD]
        out = jnp.where(sub_row2 == i, _colsum(jnp.where(diag2, t, 0.0)), out)
    return out


def _store_pieces(qbuf, slot, p, i, x):
    half = SUBLANES * HEAD
    inner = slice((p % 2) * LANES, (p % 2 + 1) * LANES)
    hi = x.astype(BF16)
    qbuf[slot, p // 2, i * HEAD:(i + 1) * HEAD, inner] = hi
    qbuf[slot, p // 2, half + i * HEAD:half + (i + 1) * HEAD, inner] = (x - hi.astype(F32)).astype(BF16)


def _left_half():
    return lax.broadcasted_iota(jnp.int32, (HEAD, LANES), 1) < HEAD


def _split_refs(refs, n_rows, n_out, exchange):
    n_in = len(exchange.operands) if exchange else 0
    n_ex_out = len(exchange.out_shapes) if exchange else 0
    refs = list(refs)
    rows, refs = refs[:n_rows], refs[n_rows:]
    ex_in, refs = refs[:n_in], refs[n_in:]
    outs, refs = refs[:n_out], refs[n_out:]
    ex_out, refs = refs[:n_ex_out], refs[n_ex_out:]
    scratch, sems = (refs[:-3], refs[-3:]) if exchange else (refs, None)
    moves = exchange.moves(ex_in, ex_out, sems) if exchange else None
    return rows, outs, scratch, moves


def _wkv_fwd(r, w, k, a, b, v, exchange=None):
    s = r.shape[0]
    tb = SCAN_TB
    nb = s // tb

    def body(*refs):
        (r_ref, w_ref, k_ref, a_ref, b_ref, v_ref), (y_ref, st_ref), (state, vbuf, qbuf), moves = _split_refs(
            refs, 6, 2, exchange)
        g = pl.program_id(0)
        if moves:
            moves.start(also=(g == 0))

        @pl.when(g == 0)
        def _():
            state[...] = jnp.zeros_like(state)
            qbuf[...] = jnp.zeros_like(qbuf)

        left = _left_half()
        diag2, bd2 = _quad_consts()
        sub_row2 = lax.broadcasted_iota(jnp.int32, (SUBLANES, 2 * LANES), 0)
        groups = tb // SUBLANES
        quads = [slice(g2 * 2 * LANES, (g2 + 1) * 2 * LANES) for g2 in range(2)]

        def rows_of(q):
            return pl.ds(pl.multiple_of(q * SUBLANES, SUBLANES), SUBLANES)

        def v_tiles(q, slot):
            v8 = v_ref[rows_of(q), :]
            for g2 in range(2):
                vbuf[slot, g2] = _rows_to_columns(v8[:, quads[g2]], diag2, bd2)

        def chain(q, slot):
            rows8 = rows_of(q)
            a8, w8, b8, k8, r8 = (x[rows8, :] for x in (a_ref, w_ref, b_ref, k_ref, r_ref))
            sp = [state[p] for p in range(N_PAIRS)]
            for i in range(SUBLANES):
                row = slice(i, i + 1)
                for p in range(N_PAIRS):
                    st_ref[q * SUBLANES + i, p] = sp[p]
                sa = [_pair_sum(sp[p] * a8[row, p * LANES:(p + 1) * LANES], left) for p in range(N_PAIRS)]
                for p in range(N_PAIRS):
                    lanes = slice(p * LANES, (p + 1) * LANES)
                    vt = vbuf[slot, p // 2, i * HEAD:(i + 1) * HEAD, (p % 2) * LANES:(p % 2 + 1) * LANES]
                    sp[p] = sp[p] * w8[row, lanes] + sa[p] * b8[row, lanes] + vt * k8[row, lanes]
                    _store_pieces(qbuf, slot, p, i, sp[p] * r8[row, lanes])
            for p in range(N_PAIRS):
                state[p] = sp[p]

        def y_rows(q, slot):
            for g2 in range(2):
                y_ref[rows_of(q), quads[g2]] = _diag_rows(qbuf[slot, g2], diag2, bd2, sub_row2)

        v_tiles(0, 0)

        def two_groups(j, carry):
            q0 = 2 * j
            v_tiles(q0 + 1, 1)
            chain(q0, 0)
            y_rows(jnp.maximum(q0 - 1, 0), 1)
            v_tiles(jnp.minimum(q0 + 2, groups - 1), 0)
            chain(q0 + 1, 1)
            y_rows(q0, 0)
            return carry

        lax.fori_loop(0, groups // 2, two_groups, 0)
        y_rows(groups - 1, 1)
        if moves:
            moves.wait(also=(g == nb - 1))

    rows = pl.BlockSpec((tb, D_HALF), lambda g: (g, 0))
    ex_in = exchange.operands if exchange else []
    ex_out = exchange.out_shapes if exchange else []
    res = pl.pallas_call(
        body, name="wkv_fwd", grid=(nb,),
        in_specs=[rows] * 6 + [ANY] * len(ex_in),
        out_specs=[rows, pl.BlockSpec((tb, N_PAIRS, HEAD, LANES), lambda g: (g, 0, 0, 0))] + [ANY] * len(ex_out),
        out_shape=[jax.ShapeDtypeStruct((s, D_HALF), F32),
                   jax.ShapeDtypeStruct((s, N_PAIRS, HEAD, LANES), F32)] + ex_out,
        scratch_shapes=[pltpu.VMEM((N_PAIRS, HEAD, LANES), F32),
                        pltpu.VMEM((2, 2, SUBLANES * HEAD, 2 * LANES), F32),
                        pltpu.VMEM((2, 2, 2 * SUBLANES * HEAD, 2 * LANES), BF16)]
                       + (exchange.scratch() if exchange else []),
        compiler_params=_params("arbitrary"),
    )(r, w, k, a, b, v, *ex_in)
    return res[0], res[1], list(res[2:])


def _wkv_bwd(r, w, k, a, b, v, dy, st, exchange=None):
    s = r.shape[0]
    tb = SCAN_TB
    nb = s // tb

    def body(*refs):
        ((r_ref, w_ref, k_ref, a_ref, b_ref, v_ref, dy_ref, st_ref),
         (dr_ref, dw_ref, dk_ref, dv_ref, da_ref, db_ref), (dstate, vbuf, qbuf), moves) = _split_refs(refs, 8, 6, exchange)
        g = pl.program_id(0)
        if moves:
            moves.start(also=(g == 0))

        @pl.when(g == 0)
        def _():
            dstate[...] = jnp.zeros_like(dstate)
            qbuf[...] = jnp.zeros_like(qbuf)

        left = _left_half()
        diag2, bd2 = _quad_consts()
        sub_row = lax.broadcasted_iota(jnp.int32, (SUBLANES, LANES), 0)
        sub_row2 = lax.broadcasted_iota(jnp.int32, (SUBLANES, 2 * LANES), 0)
        groups = tb // SUBLANES
        quads = [slice(g2 * 2 * LANES, (g2 + 1) * 2 * LANES) for g2 in range(2)]
        row_refs = (dr_ref, dw_ref, dk_ref, da_ref, db_ref)

        def rows_of(q):
            return pl.ds(pl.multiple_of(q * SUBLANES, SUBLANES), SUBLANES)

        def column_tiles(q, slot):
            rows8 = rows_of(q)
            for kind, ref in enumerate((v_ref, dy_ref)):
                x8 = ref[rows8, :]
                for g2 in range(2):
                    vbuf[slot, kind, g2] = _rows_to_columns(x8[:, quads[g2]], diag2, bd2)

        def chain(q, slot):
            rows8 = rows_of(q)
            a8, w8, b8, k8, r8 = (x[rows8, :] for x in (a_ref, w_ref, b_ref, k_ref, r_ref))
            dsp = [dstate[p] for p in range(N_PAIRS)]
            outs = [[jnp.zeros((SUBLANES, LANES), F32) for _ in row_refs] for _ in range(N_PAIRS)]
            for i in reversed(range(SUBLANES)):
                row = slice(i, i + 1)
                pl_ = [slice(p * LANES, (p + 1) * LANES) for p in range(N_PAIRS)]
                tile = [(p // 2, slice(i * HEAD, (i + 1) * HEAD), slice((p % 2) * LANES, (p % 2 + 1) * LANES))
                        for p in range(N_PAIRS)]
                sp = [st_ref[q * SUBLANES + i, p] for p in range(N_PAIRS)]
                dyt = [vbuf[(slot, 1) + tile[p]] for p in range(N_PAIRS)]
                ds = [dsp[p] + dyt[p] * r8[row, pl_[p]] for p in range(N_PAIRS)]
                dsa = [_pair_sum(ds[p] * b8[row, pl_[p]], left) for p in range(N_PAIRS)]
                sa = [_pair_sum(sp[p] * a8[row, pl_[p]], left) for p in range(N_PAIRS)]
                for p in range(N_PAIRS):
                    ar, wr, br, kr = (x[row, pl_[p]] for x in (a8, w8, b8, k8))
                    vt = vbuf[(slot, 0) + tile[p]]
                    dsp[p] = ds[p] * wr + dsa[p] * ar
                    sn = sp[p] * wr + sa[p] * br + vt * kr
                    new = (_colsum(sn * dyt[p]), _colsum(ds[p] * sp[p]), _colsum(ds[p] * vt),
                           _colsum(sp[p] * dsa[p]), _colsum(ds[p] * sa[p]))
                    outs[p] = [jnp.where(sub_row == i, n, o) for n, o in zip(new, outs[p])]
                    _store_pieces(qbuf, slot, p, i, ds[p] * kr)
            for p in range(N_PAIRS):
                dstate[p] = dsp[p]
                for ref, o in zip(row_refs, outs[p]):
                    ref[rows8, p * LANES:(p + 1) * LANES] = o

        def dv_rows(q, slot):
            for g2 in range(2):
                dv_ref[rows_of(q), quads[g2]] = _diag_rows(qbuf[slot, g2], diag2, bd2, sub_row2)

        column_tiles(groups - 1, 0)

        def two_groups(j, carry):
            q0 = groups - 1 - 2 * j
            column_tiles(q0 - 1, 1)
            chain(q0, 0)
            dv_rows(jnp.minimum(q0 + 1, groups - 1), 1)
            column_tiles(jnp.maximum(q0 - 2, 0), 0)
            chain(q0 - 1, 1)
            dv_rows(q0, 0)
            return carry

        lax.fori_loop(0, groups // 2, two_groups, 0)
        dv_rows(0, 1)
        if moves:
            moves.wait(also=(g == nb - 1))

    rows = pl.BlockSpec((tb, D_HALF), lambda g: (nb - 1 - g, 0))
    ex_in = exchange.operands if exchange else []
    ex_out = exchange.out_shapes if exchange else []
    res = pl.pallas_call(
        body, name="wkv_bwd", grid=(nb,),
        in_specs=[rows] * 7 + [pl.BlockSpec((tb, N_PAIRS, HEAD, LANES), lambda g: (nb - 1 - g, 0, 0, 0))]
                 + [ANY] * len(ex_in),
        out_specs=[rows] * 6 + [ANY] * len(ex_out),
        out_shape=[jax.ShapeDtypeStruct((s, D_HALF), F32)] * 6 + ex_out,
        scratch_shapes=[pltpu.VMEM((N_PAIRS, HEAD, LANES), F32),
                        pltpu.VMEM((2, 2, 2, SUBLANES * HEAD, 2 * LANES), F32),
                        pltpu.VMEM((2, 2, 2 * SUBLANES * HEAD, 2 * LANES), BF16)]
                       + (exchange.scratch() if exchange else []),
        compiler_params=_params("arbitrary"),
    )(r, w, k, a, b, v, dy, st, *ex_in)
    return list(res[:6]), list(res[6:])


def _rwkv_post_math(y, r, k, v, gate, lw, lb, rk, bd):
    mean = _head_sum(y, bd) * (1.0 / HEAD)
    yc = y - mean
    var = _head_sum(yc * yc, bd) * (1.0 / HEAD)
    rstd = lax.rsqrt(var + LNX_EPS)
    yn = yc * rstd
    rkk = _head_sum(r * k * rk, bd)
    sg = _sigmoid(gate)
    pre = yn * lw + lb + rkk * v
    return yn, rstd, rkk, sg, pre


def _rwkv_post(y, r, k, v, gate, lw, lb, rk, tm=256):
    s = y.shape[0]

    def body(y_ref, r_ref, k_ref, v_ref, g_ref, lw_ref, lb_ref, rk_ref, o_ref):
        gate_v = g_ref[...]
        _, _, _, sg, pre = _rwkv_post_math(y_ref[...], r_ref[...], k_ref[...], v_ref[...], gate_v,
                                           lw_ref[...], lb_ref[...], rk_ref[...], _head_ones())
        o_ref[...] = pre * (gate_v * sg)

    blk = pl.BlockSpec((tm, D_HALF), lambda i: (i, 0))
    vec = pl.BlockSpec((1, D_HALF), lambda i: (0, 0))
    return pl.pallas_call(
        body, name="rwkv_post", grid=(s // tm,),
        in_specs=[blk] * 5 + [vec] * 3, out_specs=blk,
        out_shape=jax.ShapeDtypeStruct((s, D_HALF), F32), compiler_params=_params("parallel"),
    )(y, r, k, v, gate, lw, lb, rk)


def _rwkv_post_bwd(dmix, y, r, k, v, gate, lw, lb, rk, tm=256):
    s = y.shape[0]

    def body(dm_ref, y_ref, r_ref, k_ref, v_ref, g_ref, lw_ref, lb_ref, rk_ref,
             dy_ref, dr_ref, dk_ref, dv_ref, dg_ref, dlw_ref, dlb_ref, drk_ref):
        i = pl.program_id(0)

        @pl.when(i == 0)
        def _():
            dlw_ref[...] = jnp.zeros_like(dlw_ref)
            dlb_ref[...] = jnp.zeros_like(dlb_ref)
            drk_ref[...] = jnp.zeros_like(drk_ref)

        bd = _head_ones()
        rv, kv, vv, gate_v, lw_v, rk_v = r_ref[...], k_ref[...], v_ref[...], g_ref[...], lw_ref[...], rk_ref[...]
        yn, rstd, rkk, sg, pre = _rwkv_post_math(y_ref[...], rv, kv, vv, gate_v, lw_v, lb_ref[...], rk_v, bd)
        dm = dm_ref[...]
        dg_ref[...] = dm * pre * (sg * (1.0 + gate_v * (1.0 - sg)))
        dpre = dm * (gate_v * sg)
        dlw_ref[...] += _colsum(dpre * yn)
        dlb_ref[...] += _colsum(dpre)
        dyn = dpre * lw_v
        m1 = _head_sum(dyn, bd) * (1.0 / HEAD)
        m2 = _head_sum(dyn * yn, bd) * (1.0 / HEAD)
        dy_ref[...] = rstd * (dyn - m1 - yn * m2)
        dv_ref[...] = dpre * rkk
        drkk = _head_sum(dpre * vv, bd)
        dr_ref[...] = drkk * kv * rk_v
        dk_ref[...] = drkk * rv * rk_v
        drk_ref[...] += _colsum(drkk * rv * kv)

    blk = pl.BlockSpec((tm, D_HALF), lambda i: (i, 0))
    vec = pl.BlockSpec((1, D_HALF), lambda i: (0, 0))
    return pl.pallas_call(
        body, name="rwkv_post_bwd", grid=(s // tm,),
        in_specs=[blk] * 6 + [vec] * 3, out_specs=[blk] * 5 + [vec] * 3,
        out_shape=[jax.ShapeDtypeStruct((s, D_HALF), F32)] * 5 + [jax.ShapeDtypeStruct((1, D_HALF), F32)] * 3,
        compiler_params=_params("arbitrary"),
    )(dmix, y, r, k, v, gate, lw, lb, rk)


def _rwkv_prep_bwd(u_a, grads, mu, wl, w0, a0, kkw, kaw, tm=256):
    s = u_a.shape[0]
    nb = s // tm

    def body(ua_ref, prev_ref, drs_ref, dws_ref, dks_ref, dvs_ref, das_ref, dbs_ref, drb_ref, dkb_ref, dvb_ref,
             dgt_ref, mu_ref, wl_ref, w0_ref, a0_ref, kkw_ref, kaw_ref,
             du_ref, dmu_ref, dwl_ref, dw0_ref, da0_ref, dkkw_ref, dkaw_ref, carry):
        i = pl.program_id(0)

        @pl.when(i == 0)
        def _():
            carry[...] = jnp.zeros_like(carry)
            for ref in (dmu_ref, dwl_ref, dw0_ref, da0_ref, dkkw_ref, dkaw_ref):
                ref[...] = jnp.zeros_like(ref)

        bd = _head_ones()
        mu_v, wl_v, kkw_v, kaw_v = mu_ref[...], wl_ref[...], kkw_ref[...], kaw_ref[...]
        f = _rwkv_elementwise(ua_ref[...], prev_ref[7:8, :], i == nb - 1, mu_v, wl_v, w0_ref[...],
                              a0_ref[...], kkw_v, kaw_v, bd)
        a, kk, k0 = f["a"], f["kk"], f["k0"]
        dk = dks_ref[...] + dkb_ref[...]
        dbs = dbs_ref[...]
        dkk = dbs * a - das_ref[...]
        da = dbs * kk + dk * k0 * kaw_v
        dk0 = dk * (1.0 + (a - 1.0) * kaw_v)
        dkaw_ref[...] += _colsum(dk * k0 * (a - 1.0))
        inv = 1.0 / f["nrm"]
        proj = _head_sum(dkk * kk, bd)
        dkk0 = jnp.where(f["ss"] > 1e-24, (dkk - kk * proj) * inv, dkk * inv)
        dk0 = dk0 + dkk0 * kkw_v
        dkkw_ref[...] += _colsum(dkk0 * k0)
        dza = da * a * (1.0 - a)
        da0_ref[...] += _colsum(dza)
        dz = -dws_ref[...] * f["dec"] * f["e"] * (1.0 - f["sz"])
        dw0_ref[...] += _colsum(dz)
        dll = jnp.concatenate([dz, dza], axis=1).astype(BF16)
        dwl_ref[...] += _dot_tn(f["lin"].astype(BF16), dll)
        dlin = _dot_nt(dll, wl_v)
        lane = lax.broadcasted_iota(jnp.int32, (1, LANES), 1)
        th = f["th"]
        dlo = jnp.where(lane < LORA, dlin * (1.0 - th * th), dlin)
        dus = jnp.concatenate([drs_ref[...] + drb_ref[...], dk0, dvs_ref[...] + dvb_ref[...], dlo, dgt_ref[...]],
                              axis=1)
        dmu_ref[...] += _colsum(dus * f["delta"])
        g1 = dus * mu_v
        rows = lax.broadcasted_iota(jnp.int32, (tm, 1), 0)
        up = jnp.where(rows == tm - 1, carry[...], pltpu.roll(g1, tm - 1, 0))
        du_ref[...] = dus - g1 + up
        carry[...] = g1[0:1, :]

    rev = lambda w: pl.BlockSpec((tm, w), lambda i: (nb - 1 - i, 0))
    vec = lambda w: pl.BlockSpec((1, w), lambda i: (0, 0))
    wl_spec = pl.BlockSpec((LANES, 2 * D_HALF), lambda i: (0, 0))
    return pl.pallas_call(
        body, name="rwkv_prep_bwd", grid=(nb,),
        in_specs=[rev(SEC), pl.BlockSpec((8, SEC), lambda i: (jnp.maximum((nb - 1 - i) * (tm // 8) - 1, 0), 0))]
                 + [rev(D_HALF)] * 10 + [vec(SEC), wl_spec] + [vec(D_HALF)] * 4,
        out_specs=[rev(SEC), vec(SEC), wl_spec] + [vec(D_HALF)] * 4,
        out_shape=[jax.ShapeDtypeStruct((s, SEC), F32), jax.ShapeDtypeStruct((1, SEC), F32),
                   jax.ShapeDtypeStruct((LANES, 2 * D_HALF), F32)] + [jax.ShapeDtypeStruct((1, D_HALF), F32)] * 4,
        scratch_shapes=[pltpu.VMEM((1, SEC), F32)],
        compiler_params=_params("arbitrary"),
    )(u_a, u_a, *grads, mu, wl, w0, a0, kkw, kaw)


def _tri(tm, lower):
    r = lax.broadcasted_iota(jnp.int32, (tm, tm), 0)
    c = lax.broadcasted_iota(jnp.int32, (tm, tm), 1)
    return ((r >= c) if lower else (r <= c)).astype(BF16)


def _head_rms(x, g, bd):
    rinv = lax.rsqrt(_head_sum(x * x, bd) * (1.0 / HEAD) + RMS_EPS)
    xh = x * rinv
    return xh, rinv, xh * g


def _fox_prep(u_b, fb, qg, kg, tm=256):
    s = u_b.shape[0]

    def body(ub_ref, fb_ref, qg_ref, kg_ref, q_ref, k_ref, v_ref, cc_ref, cr_ref, carry):
        i = pl.program_id(0)

        @pl.when(i == 0)
        def _():
            carry[...] = jnp.zeros_like(carry)

        bd = _head_ones()
        _, _, qn = _head_rms(ub_ref[:, 0:512], qg_ref[...], bd)
        _, _, kn = _head_rms(ub_ref[:, 512:1024], kg_ref[...], bd)
        q_ref[...] = (qn * ATT_SCALE).astype(BF16)
        k_ref[...] = kn.astype(BF16)
        v_ref[...] = ub_ref[:, 1024:1536].astype(BF16)
        lane = lax.broadcasted_iota(jnp.int32, (1, LANES), 1)
        logf = jnp.where(lane < N_HEADS, _log_sigmoid(ub_ref[:, 2048:2176] + fb_ref[...]), 0.0)
        cum = _exact_dot(logf, _tri(tm, True), ones_first=True) + carry[...]
        for h in range(N_HEADS):
            cc_ref[h] = jnp.broadcast_to(cum[:, h:h + 1], (tm, LANES))
        cr_ref[...] = jnp.transpose(cum)[0:N_HEADS, :]
        carry[...] = cum[tm - 1:tm, :]

    blk = pl.BlockSpec((tm, D_HALF), lambda i: (i, 0))
    return pl.pallas_call(
        body, name="fox_prep", grid=(s // tm,),
        in_specs=[pl.BlockSpec((tm, SEC), lambda i: (i, 0)), pl.BlockSpec((1, LANES), lambda i: (0, 0)),
                  pl.BlockSpec((1, D_HALF), lambda i: (0, 0)), pl.BlockSpec((1, D_HALF), lambda i: (0, 0))],
        out_specs=[blk, blk, blk, pl.BlockSpec((N_HEADS, tm, LANES), lambda i: (0, i, 0)),
                   pl.BlockSpec((N_HEADS, tm), lambda i: (0, i))],
        out_shape=[jax.ShapeDtypeStruct((s, D_HALF), BF16)] * 3
                  + [jax.ShapeDtypeStruct((N_HEADS, s, LANES), F32), jax.ShapeDtypeStruct((N_HEADS, s), F32)],
        scratch_shapes=[pltpu.VMEM((1, LANES), F32)],
        compiler_params=_params("arbitrary"),
    )(u_b, fb, qg, kg)


ATT_T = 256


def _attn_fwd(q, k, v, cc, cr, u_b):
    s = q.shape[0]
    t = ATT_T
    nblk = s // t

    def body(q_ref, k_ref, v_ref, cc_ref, cr_ref, g_ref, o_ref, mix_ref, lse_ref, m_sc, l_sc, acc_sc):
        i = pl.program_id(0)
        j = pl.program_id(1)

        @pl.when(j == 0)
        def _():
            m_sc[...] = jnp.full_like(m_sc, NEG)
            l_sc[...] = jnp.zeros_like(l_sc)
            acc_sc[...] = jnp.zeros_like(acc_sc)

        @pl.when(j <= i)
        def _():
            row = i * t + lax.broadcasted_iota(jnp.int32, (t, t), 0)
            col = j * t + lax.broadcasted_iota(jnp.int32, (t, t), 1)
            causal = row >= col
            left = lax.broadcasted_iota(jnp.int32, (1, LANES), 1) < HEAD
            for p in range(N_PAIRS):
                lanes = slice(p * LANES, (p + 1) * LANES)
                q2, k2, v2 = q_ref[:, lanes], k_ref[:, lanes], v_ref[:, lanes]
                acc2 = acc_sc[:, lanes]
                for e in range(2):
                    h = 2 * p + e
                    msk = left if e == 0 else jnp.logical_not(left)
                    sc = _dot_nt(jnp.where(msk, q2, jnp.zeros_like(q2)), k2)
                    sc = sc + (_wide(cc_ref[h]) - cr_ref[h:h + 1, :])
                    sc = jnp.where(causal, sc, NEG)
                    m_prev = m_sc[h]
                    m_new = jnp.maximum(m_prev, jnp.max(sc, axis=1, keepdims=True))
                    alpha = jnp.exp(m_prev - m_new)
                    pm = jnp.exp(sc - _wide(m_new))
                    l_sc[h] = alpha * l_sc[h] + jnp.sum(pm, axis=1, keepdims=True)
                    m_sc[h] = m_new
                    pv = jnp.dot(pm.astype(BF16), v2, preferred_element_type=F32)
                    acc2 = jnp.where(msk, alpha * acc2 + pv, acc2)
                acc_sc[:, lanes] = acc2

        @pl.when(j == i)
        def _():
            left = lax.broadcasted_iota(jnp.int32, (1, LANES), 1) < HEAD
            for p in range(N_PAIRS):
                lanes = slice(p * LANES, (p + 1) * LANES)
                inv = jnp.where(left, 1.0 / l_sc[2 * p], 1.0 / l_sc[2 * p + 1])
                o = acc_sc[:, lanes] * inv
                o_ref[:, lanes] = o
                gate = g_ref[:, lanes]
                mix_ref[:, lanes] = o * (gate * _sigmoid(gate))
            for h in range(N_HEADS):
                lse_ref[h] = m_sc[h] + jnp.log(l_sc[h])

    qblk = pl.BlockSpec((t, D_HALF), lambda i, j: (i, 0))
    kblk = pl.BlockSpec((t, D_HALF), lambda i, j: (jnp.minimum(i, j), 0))
    return pl.pallas_call(
        body, name="fox_attn_fwd", grid=(nblk, nblk),
        in_specs=[qblk, kblk, kblk, pl.BlockSpec((N_HEADS, t, LANES), lambda i, j: (0, i, 0)),
                  pl.BlockSpec((N_HEADS, t), lambda i, j: (0, jnp.minimum(i, j))),
                  pl.BlockSpec((t, D_HALF), lambda i, j: (i, 3))],
        out_specs=[qblk, qblk, pl.BlockSpec((N_HEADS, t, LANES), lambda i, j: (0, i, 0))],
        out_shape=[jax.ShapeDtypeStruct((s, D_HALF), F32), jax.ShapeDtypeStruct((s, D_HALF), F32),
                   jax.ShapeDtypeStruct((N_HEADS, s, LANES), F32)],
        scratch_shapes=[pltpu.VMEM((N_HEADS, t, LANES), F32), pltpu.VMEM((N_HEADS, t, LANES), F32),
                        pltpu.VMEM((t, D_HALF), F32)],
        compiler_params=_params("parallel", "arbitrary"),
    )(q, k, v, cc, cr, u_b)


def _fox_post_bwd(dmix, o, u_b, tm=256):
    s = o.shape[0]

    def body(dm_ref, o_ref, g_ref, do_ref, dg_ref):
        gate = g_ref[...]
        sg = _sigmoid(gate)
        dm = dm_ref[...]
        do_ref[...] = (dm * (gate * sg)).astype(BF16)
        dg_ref[...] = dm * o_ref[...] * (sg * (1.0 + gate * (1.0 - sg)))

    blk = pl.BlockSpec((tm, D_HALF), lambda i: (i, 0))
    return pl.pallas_call(
        body, name="fox_post_bwd", grid=(s // tm,),
        in_specs=[blk, blk, pl.BlockSpec((tm, D_HALF), lambda i: (i, 3))], out_specs=[blk] * 2,
        out_shape=[jax.ShapeDtypeStruct((s, D_HALF), BF16), jax.ShapeDtypeStruct((s, D_HALF), F32)],
        compiler_params=_params("parallel"),
    )(dmix, o, u_b)


def _wide(x):
    return jnp.concatenate([x, x], axis=1)


def _attn_probs(q2, k2, v2, do2, msk, causal, bias, lse_rows):
    zero = jnp.zeros_like(q2)
    qh = jnp.where(msk, q2, zero)
    doh = jnp.where(msk, do2, zero)
    sc = jnp.where(causal, _dot_nt(qh, k2) + bias, NEG)
    pm = jnp.exp(sc - _wide(lse_rows))
    dp = _dot_nt(doh, v2)
    return qh, doh, pm, dp


def _attn_bwd_rowdot(q, k, v, do, lse, cc, cr):
    s = q.shape[0]
    t = ATT_T
    nblk = s // t

    def body(q_ref, k_ref, v_ref, do_ref, lse_ref, cc_ref, cr_ref, dd_ref, acc):
        i = pl.program_id(0)
        j = pl.program_id(1)

        @pl.when(j == 0)
        def _():
            acc[...] = jnp.zeros_like(acc)

        @pl.when(j <= i)
        def _():
            row = i * t + lax.broadcasted_iota(jnp.int32, (t, t), 0)
            col = j * t + lax.broadcasted_iota(jnp.int32, (t, t), 1)
            causal = row >= col
            left = lax.broadcasted_iota(jnp.int32, (1, LANES), 1) < HEAD
            for p in range(N_PAIRS):
                lanes = slice(p * LANES, (p + 1) * LANES)
                q2, k2, v2, do2 = q_ref[:, lanes], k_ref[:, lanes], v_ref[:, lanes], do_ref[:, lanes]
                for e in range(2):
                    h = 2 * p + e
                    msk = left if e == 0 else jnp.logical_not(left)
                    bias = _wide(cc_ref[h]) - cr_ref[h:h + 1, :]
                    _, _, pm, dp = _attn_probs(q2, k2, v2, do2, msk, causal, bias, lse_ref[h])
                    acc[h] += jnp.sum(pm * dp, axis=1, keepdims=True)

        @pl.when(j == i)
        def _():
            dd_ref[...] = acc[...]

    qblk = pl.BlockSpec((t, D_HALF), lambda i, j: (i, 0))
    qcol = pl.BlockSpec((N_HEADS, t, LANES), lambda i, j: (0, i, 0))
    kblk = pl.BlockSpec((t, D_HALF), lambda i, j: (jnp.minimum(i, j), 0))
    return pl.pallas_call(
        body, name="fox_attn_rowdot", grid=(nblk, nblk),
        in_specs=[qblk, kblk, kblk, qblk, qcol, qcol, pl.BlockSpec((N_HEADS, t), lambda i, j: (0, jnp.minimum(i, j)))],
        out_specs=qcol, out_shape=jax.ShapeDtypeStruct((N_HEADS, s, LANES), F32),
        scratch_shapes=[pltpu.VMEM((N_HEADS, t, LANES), F32)],
        compiler_params=_params("parallel", "arbitrary"),
    )(q, k, v, do, lse, cc, cr)


def _attn_bwd(q, k, v, do, lse, dd, cc, cr):
    s = q.shape[0]
    t = ATT_T
    nblk = s // t

    def body(q_ref, k_ref, v_ref, do_ref, lse_ref, dd_ref, cc_ref, cr_ref,
             dq_ref, dk_ref, dv_ref, dcr_ref, dk_sc, dv_sc, dcr_sc):
        j = pl.program_id(0)
        i = pl.program_id(1)

        @pl.when(jnp.logical_and(j == 0, i == 0))
        def _():
            dq_ref[...] = jnp.zeros_like(dq_ref)

        @pl.when(i == 0)
        def _():
            dk_sc[...] = jnp.zeros_like(dk_sc)
            dv_sc[...] = jnp.zeros_like(dv_sc)
            dcr_sc[...] = jnp.zeros_like(dcr_sc)

        @pl.when(i >= j)
        def _():
            row = i * t + lax.broadcasted_iota(jnp.int32, (t, t), 0)
            col = j * t + lax.broadcasted_iota(jnp.int32, (t, t), 1)
            causal = row >= col
            left = lax.broadcasted_iota(jnp.int32, (1, LANES), 1) < HEAD
            qrows = pl.ds(pl.multiple_of(i * t, t), t)
            for p in range(N_PAIRS):
                lanes = slice(p * LANES, (p + 1) * LANES)
                q2, k2, v2, do2 = q_ref[:, lanes], k_ref[:, lanes], v_ref[:, lanes], do_ref[:, lanes]
                zero = jnp.zeros_like(q2)
                dq2 = jnp.zeros((t, LANES), F32)
                dk2 = jnp.zeros((t, LANES), F32)
                dv2 = jnp.zeros((t, LANES), F32)
                for e in range(2):
                    h = 2 * p + e
                    msk = left if e == 0 else jnp.logical_not(left)
                    bias = _wide(cc_ref[h]) - cr_ref[h:h + 1, :]
                    qh, doh, pm, dp = _attn_probs(q2, k2, v2, do2, msk, causal, bias, lse_ref[h])
                    dsc = pm * (dp - _wide(dd_ref[h]))
                    dsb = dsc.astype(BF16)
                    dv2 += _dot_tn(pm.astype(BF16), doh)
                    dk2 += _dot_tn(dsb, qh)
                    dq2 += jnp.dot(dsb, jnp.where(msk, k2, zero), preferred_element_type=F32)
                    dcr_sc[h:h + 1, :] += -_colsum(dsc)
                dq_ref[qrows, lanes] += dq2 * ATT_SCALE
                dk_sc[:, lanes] += dk2
                dv_sc[:, lanes] += dv2

        @pl.when(i == nblk - 1)
        def _():
            dk_ref[...] = dk_sc[...]
            dv_ref[...] = dv_sc[...]
            dcr_ref[...] = dcr_sc[...]

    qblk = pl.BlockSpec((t, D_HALF), lambda j, i: (jnp.maximum(i, j), 0))
    qcol = pl.BlockSpec((N_HEADS, t, LANES), lambda j, i: (0, jnp.maximum(i, j), 0))
    kblk = pl.BlockSpec((t, D_HALF), lambda j, i: (j, 0))
    return pl.pallas_call(
        body, name="fox_attn_bwd", grid=(nblk, nblk),
        in_specs=[qblk, kblk, kblk, qblk, qcol, qcol, qcol, pl.BlockSpec((N_HEADS, t), lambda j, i: (0, j))],
        out_specs=[pl.BlockSpec((s, D_HALF), lambda j, i: (0, 0)), kblk, kblk,
                   pl.BlockSpec((N_HEADS, t), lambda j, i: (0, j))],
        out_shape=[jax.ShapeDtypeStruct((s, D_HALF), F32)] * 3 + [jax.ShapeDtypeStruct((N_HEADS, s), F32)],
        scratch_shapes=[pltpu.VMEM((t, D_HALF), F32), pltpu.VMEM((t, D_HALF), F32), pltpu.VMEM((N_HEADS, t), F32)],
        compiler_params=_params("arbitrary", "arbitrary"),
    )(q, k, v, do, lse, dd, cc, cr)


def _fox_prep_bwd(u_b, dq, dk, dv, dgate, dcum, fb, qg, kg, tm=256):
    s = u_b.shape[0]
    nb = s // tm

    def body(ub_ref, dq_ref, dk_ref, dv_ref, dg_ref, dc_ref, fb_ref, qg_ref, kg_ref,
             du_ref, dqg_ref, dkg_ref, dfb_ref, carry):
        i = pl.program_id(0)

        @pl.when(i == 0)
        def _():
            carry[...] = jnp.zeros_like(carry)
            dqg_ref[...] = jnp.zeros_like(dqg_ref)
            dkg_ref[...] = jnp.zeros_like(dkg_ref)
            dfb_ref[...] = jnp.zeros_like(dfb_ref)

        bd = _head_ones()
        for lo, g_ref, d_ref, dgain_ref in ((0, qg_ref, dq_ref, dqg_ref), (512, kg_ref, dk_ref, dkg_ref)):
            gain = g_ref[...]
            xh, rinv, _ = _head_rms(ub_ref[:, lo:lo + 512], gain, bd)
            dn = d_ref[...]
            dgain_ref[...] += _colsum(dn * xh)
            dxh = dn * gain
            du_ref[:, lo:lo + 512] = rinv * (dxh - xh * (_head_sum(dxh * xh, bd) * (1.0 / HEAD)))
        du_ref[:, 1024:1536] = dv_ref[...]
        du_ref[:, 1536:2048] = dg_ref[...]
        lane = lax.broadcasted_iota(jnp.int32, (1, LANES), 1)
        dc = dc_ref[...]
        dlogf = _exact_dot(dc, _tri(tm, False), ones_first=True) + carry[...]
        carry[...] += _colsum(dc)
        fl = ub_ref[:, 2048:2176] + fb_ref[...]
        dfl = jnp.where(lane < N_HEADS, dlogf * (1.0 - _sigmoid(fl)), 0.0)
        du_ref[:, 2048:2176] = dfl
        dfb_ref[...] += _colsum(dfl)

    rev = lambda w: pl.BlockSpec((tm, w), lambda i: (nb - 1 - i, 0))
    vec = lambda w: pl.BlockSpec((1, w), lambda i: (0, 0))
    return pl.pallas_call(
        body, name="fox_prep_bwd", grid=(nb,),
        in_specs=[rev(SEC)] + [rev(D_HALF)] * 4 + [rev(LANES), vec(LANES), vec(D_HALF), vec(D_HALF)],
        out_specs=[rev(SEC), vec(D_HALF), vec(D_HALF), vec(LANES)],
        out_shape=[jax.ShapeDtypeStruct((s, SEC), F32), jax.ShapeDtypeStruct((1, D_HALF), F32),
                   jax.ShapeDtypeStruct((1, D_HALF), F32), jax.ShapeDtypeStruct((1, LANES), F32)],
        scratch_shapes=[pltpu.VMEM((1, LANES), F32)],
        compiler_params=_params("arbitrary"),
    )(u_b, dq, dk, dv, dgate, dcum, fb, qg, kg)


def _merge(mix_a, mix_b, u_g, x, tgt, wa, wb, wo, fg, tm=256):
    s, d = x.shape

    def body(ma_ref, mb_ref, ug_ref, x_ref, t_ref, wa_ref, wb_ref, wo_ref, fg_ref,
             dx2_ref, dma_ref, dmb_ref, dug_ref, dwa_ref, dwb_ref, dwo_ref, dfg_ref, loss_ref):
        i = pl.program_id(0)

        @pl.when(i == 0)
        def _():
            for ref in (dwa_ref, dwb_ref, dwo_ref, dfg_ref, loss_ref):
                ref[...] = jnp.zeros_like(ref)

        wa_v, wb_v, wo_v, fg_v = wa_ref[...], wb_ref[...], wo_ref[...], fg_ref[...]
        ma = ma_ref[...].astype(BF16)
        mb = mb_ref[...].astype(BF16)
        ya = jnp.dot(ma, wa_v, preferred_element_type=F32)
        yb = jnp.dot(mb, wb_v, preferred_element_type=F32)
        sa = _sigmoid(ug_ref[:, 0:d])
        sb = _sigmoid(ug_ref[:, d:2 * d])
        merged = (sa * ya + sb * yb).astype(BF16)
        x2 = x_ref[...] + jnp.dot(merged, wo_v, preferred_element_type=F32)
        r2 = lax.rsqrt(jnp.mean(x2 * x2, axis=-1, keepdims=True) + RMS_EPS)
        x2h = x2 * r2
        err = x2h * fg_v - t_ref[...]
        loss_ref[...] += _colsum(err * err)
        dy = err * (1.0 / d)
        dfg_ref[...] += _colsum(dy * x2h)
        dx2h = dy * fg_v
        dx2 = r2 * (dx2h - x2h * jnp.mean(dx2h * x2h, axis=-1, keepdims=True))
        dx2_ref[...] = dx2
        dx2b = dx2.astype(BF16)
        dmerged = _dot_nt(dx2b, wo_v)
        dwo_ref[...] += _dot_tn(merged, dx2b)
        dya = dmerged * sa
        dyb = dmerged * sb
        dug_ref[:, 0:d] = dya * ya * (1.0 - sa)
        dug_ref[:, d:2 * d] = dyb * yb * (1.0 - sb)
        dyab = dya.astype(BF16)
        dybb = dyb.astype(BF16)
        dma_ref[...] = _dot_nt(dyab, wa_v)
        dmb_ref[...] = _dot_nt(dybb, wb_v)
        dwa_ref[...] += _dot_tn(ma, dyab)
        dwb_ref[...] += _dot_tn(mb, dybb)

    row = lambda w: pl.BlockSpec((tm, w), lambda i: (i, 0))
    full = lambda a: pl.BlockSpec(a.shape, lambda i: (0, 0))
    fshape = lambda a: jax.ShapeDtypeStruct(a.shape, F32)
    return pl.pallas_call(
        body, name="merge_fwd_bwd", grid=(s // tm,),
        in_specs=[row(D_HALF), row(D_HALF), row(GATE_COLS), row(d), row(d), full(wa), full(wb), full(wo), full(fg)],
        out_specs=[row(d), row(D_HALF), row(D_HALF), row(GATE_COLS), full(wa), full(wb), full(wo), full(fg), full(fg)],
        out_shape=[jax.ShapeDtypeStruct((s, d), F32), jax.ShapeDtypeStruct((s, D_HALF), F32),
                   jax.ShapeDtypeStruct((s, D_HALF), F32), jax.ShapeDtypeStruct((s, GATE_COLS), F32),
                   fshape(wa), fshape(wb), fshape(wo), fshape(fg), fshape(fg)],
        compiler_params=_params("arbitrary"),
    )(mix_a, mix_b, u_g, x, tgt, wa, wb, wo, fg)


def _lora_weight(w_up, a_up):
    z = jnp.zeros((LORA, D_HALF), w_up.dtype)
    return jnp.concatenate([jnp.concatenate([w_up, z], axis=1), jnp.concatenate([z, a_up], axis=1)], axis=0)


def _device_grads(x, tgt, p, w_a, w_up, a_up, late_weights, fwd_exchange=None, bwd_exchange=None):
    wl = _lora_weight(w_up, a_up)
    rk = p["r_k"].reshape(1, D_HALF)
    fb = jnp.pad(p["f_bias"], ((0, 0), (0, LANES - N_HEADS)))
    qg = jnp.tile(p["q_norm_g"], (1, N_HEADS))
    kg = jnp.tile(p["k_norm_g"], (1, N_HEADS))
    fg = p["final_norm_g"].reshape(1, D_MODEL)
    mixer = (p["shift_mu"], wl, p["w0"], p["a0"], p["k_k"], p["k_a"])

    h = _rmsnorm_in(x, p["norm_g"])
    u_a = _matmul_nn(h, w_a, "inproj_rwkv")
    r, dec, k, v, av, bv, gate_a = _rwkv_prep(u_a, *mixer)
    y, st, arrived = _wkv_fwd(r, dec, k, av, bv, v, fwd_exchange)
    mix_a = _rwkv_post(y, r, k, v, gate_a, p["lnx_w"], p["lnx_b"], rk)

    w_b, w_g, w_out_a, w_out_b, w_out = late_weights(arrived)
    u_b = _matmul_nn(h, w_b, "inproj_fox")
    u_g = _matmul_nn(h, w_g, "inproj_gate")
    q, kn, vb, cc, cr = _fox_prep(u_b, fb, qg, kg)
    o, mix_b, lse = _attn_fwd(q, kn, vb, cc, cr, u_b)

    dx2, dmix_a, dmix_b, du_g, dwa, dwb, dwo, dfg, loss_vec = _merge(
        mix_a, mix_b, u_g, x, tgt, w_out_a, w_out_b, w_out, fg)

    do, dgate_b = _fox_post_bwd(dmix_b, o, u_b)
    dd = _attn_bwd_rowdot(q, kn, vb, do, lse, cc, cr)
    dq, dk_att, dv_att, dcr = _attn_bwd(q, kn, vb, do, lse, dd, cc, cr)
    dcum = jnp.pad(dcr.T, ((0, 0), (0, LANES - N_HEADS)))
    du_b, dqg, dkg, dfb = _fox_prep_bwd(u_b, dq, dk_att, dv_att, dgate_b, dcum, fb, qg, kg)
    h_t = h.T
    dw_b = _matmul_tn_acc(h_t, du_b, "dw_fox")
    dw_g = _matmul_tn_acc(h_t, du_g, "dw_gate")

    dy, dr_b, dk_b, dv_b, dgate_a, dlw, dlb, drk = _rwkv_post_bwd(
        dmix_a, y, r, k, v, gate_a, p["lnx_w"], p["lnx_b"], rk)
    scan_grads, sent = _wkv_bwd(r, dec, k, av, bv, v, dy, st,
                                bwd_exchange(dw_b, dw_g, dwa, dwb, dwo) if bwd_exchange else None)
    du_a, dmu, dwl, dw0, da0, dkkw, dkaw = _rwkv_prep_bwd(u_a, (*scan_grads, dr_b, dk_b, dv_b, dgate_a), *mixer)
    dw_a = _matmul_tn_acc(h_t, du_a, "dw_rwkv")
    grad_x, dnorm_g = _inproj_bwd(du_a, du_b, du_g, w_a, w_b, w_g, x, dx2, p["norm_g"])

    grads = dict(
        norm_g=dnorm_g, w_in=(dw_a, dw_b, dw_g), shift_mu=dmu,
        w_lora_up=dwl[:LORA, :D_HALF], w0=dw0, a_lora_up=dwl[LORA:, D_HALF:], a0=da0, k_k=dkkw, k_a=dkaw,
        r_k=drk.reshape(1, N_HEADS, HEAD), lnx_w=dlw, lnx_b=dlb, f_bias=dfb[:, :N_HEADS],
        q_norm_g=dqg.reshape(N_HEADS, HEAD).sum(axis=0, keepdims=True),
        k_norm_g=dkg.reshape(N_HEADS, HEAD).sum(axis=0, keepdims=True),
        w_out_a=dwa, w_out_b=dwb, w_out=dwo, final_norm_g=dfg.reshape(D_MODEL))
    return loss_vec, grad_x, grads, sent


CHIP_FLIPS = ((1, 0), (0, 1), (1, 1))
ANY = pl.BlockSpec(memory_space=pl.ANY)


def _position():
    return lax.axis_index("x"), lax.axis_index("y"), lax.axis_index("c")


def _flip(v, f):
    return 1 - v if f else v


def _both(a, b):
    if a is None:
        return b
    return a if b is None else jnp.logical_and(a, b)


def _when(cond, fn):
    if cond is None:
        fn()
    else:
        pl.when(cond)(fn)


class _Moves:
    def __init__(self, send_sems, recv_sems, local_sems):
        self.send_sems, self.recv_sems, self.local_sems = send_sems, recv_sems, local_sems
        self.remote, self.local = [], []

    def send(self, src, dst, peer, landing, send_if=None, recv_if=None):
        k = len(self.remote)
        sems = dict(send_sem=self.send_sems.at[k], recv_sem=self.recv_sems.at[k], device_id=peer, device_id_type=MESH)
        out = pltpu.make_async_remote_copy(src_ref=src, dst_ref=dst, **sems)
        arrival = pltpu.make_async_remote_copy(src_ref=src, dst_ref=landing, **sems)
        self.remote.append((out, arrival, send_if, recv_if))

    def copy(self, src, dst, cond=None):
        cp = pltpu.make_async_copy(src, dst, self.local_sems.at[len(self.local)])
        self.local.append((cp, cond))

    def start(self, also=None):
        for cp, cond in self.local:
            _when(_both(also, cond), cp.start)
        for out, _, send_if, _ in self.remote:
            _when(_both(also, send_if), out.start)

    def wait(self, also=None):
        for _, arrival, _, recv_if in self.remote:
            _when(_both(also, recv_if), arrival.wait_recv)
        for out, _, send_if, _ in self.remote:
            _when(_both(also, send_if), out.wait_send)
        for cp, cond in self.local:
            _when(_both(also, cond), cp.wait)


class _Exchange:
    def __init__(self, operands, out_shapes, n_remote, n_local, build):
        self.operands, self.out_shapes = list(operands), list(out_shapes)
        self.n_remote, self.n_local, self.build = n_remote, n_local, build

    def scratch(self):
        return [pltpu.SemaphoreType.DMA((self.n_remote,)), pltpu.SemaphoreType.DMA((self.n_remote,)),
                pltpu.SemaphoreType.DMA((max(self.n_local, 1),))]

    def moves(self, in_refs, out_refs, sems):
        mv = _Moves(*sems)
        self.build(mv, in_refs, out_refs)
        return mv

    def run_alone(self, name):
        n_in, n_out = len(self.operands), len(self.out_shapes)

        def body(*refs):
            mv = self.moves(refs[:n_in], refs[n_in:n_in + n_out], refs[n_in + n_out:])
            mv.start()
            mv.wait()

        return pl.pallas_call(
            body, name=name, in_specs=[ANY] * n_in, out_specs=[ANY] * n_out, out_shape=self.out_shapes,
            scratch_shapes=self.scratch(), compiler_params=pltpu.CompilerParams(has_side_effects=True),
        )(*self.operands)


def _gather_exchange(half, w_in_shard, others):
    n = len(others)

    def build(mv, ins, outs):
        x, y, c = _position()
        me = 2 * x + y
        mv.copy(ins[0], outs[0].at[y], cond=(x == half))
        for t in range(n):
            mv.copy(ins[1 + t], outs[1 + t].at[me])
        for fx, fy in CHIP_FLIPS:
            px, py = _flip(x, fx), _flip(y, fy)
            peer = (px, py, c)
            mv.send(ins[0], outs[0].at[y], peer, landing=outs[0].at[py], send_if=(x == half), recv_if=(px == half))
            for t in range(n):
                mv.send(ins[1 + t], outs[1 + t].at[me], peer, landing=outs[1 + t].at[2 * px + py])

    shapes = [jax.ShapeDtypeStruct((2,) + w_in_shard.shape, w_in_shard.dtype)]
    shapes += [jax.ShapeDtypeStruct((N_CHIPS,) + a.shape, a.dtype) for a in others]
    return _Exchange([w_in_shard] + list(others), shapes, len(CHIP_FLIPS) * (1 + n), 1 + n, build)


def _scatter_exchange(half, w_in_blocks, stacks):
    n = len(stacks)

    def build(mv, ins, outs):
        x, y, c = _position()
        for f, (fx, fy) in enumerate(CHIP_FLIPS):
            px, py = _flip(x, fx), _flip(y, fy)
            peer = (px, py, c)
            mv.send(ins[0].at[py], outs[0].at[f], peer, landing=outs[0].at[f], send_if=(px == half), recv_if=(x == half))
            for t in range(n):
                mv.send(ins[1 + t].at[2 * px + py], outs[1 + t].at[f], peer, landing=outs[1 + t].at[f])

    shapes = [jax.ShapeDtypeStruct((len(CHIP_FLIPS),) + a.shape[1:], a.dtype) for a in [w_in_blocks] + list(stacks)]
    return _Exchange([w_in_blocks] + list(stacks), shapes, len(CHIP_FLIPS) * (1 + n), 0, build)


def _swap_sibling(tensors):
    n = len(tensors)

    def body(*refs):
        ins, outs = refs[:n], refs[n:2 * n]
        send_sems, recv_sems = refs[2 * n:]
        x, y, c = _position()
        copies = [pltpu.make_async_remote_copy(
            src_ref=ins[t], dst_ref=outs[t], send_sem=send_sems.at[t], recv_sem=recv_sems.at[t],
            device_id=(x, y, 1 - c), device_id_type=MESH) for t in range(n)]
        for cp in copies:
            cp.start()
        for cp in copies:
            cp.wait_recv()
        for cp in copies:
            cp.wait_send()

    return pl.pallas_call(
        body, name="swap_sibling", in_specs=[ANY] * n, out_specs=[ANY] * n,
        out_shape=[jax.ShapeDtypeStruct(a.shape, a.dtype) for a in tensors],
        scratch_shapes=[pltpu.SemaphoreType.DMA((n,)), pltpu.SemaphoreType.DMA((n,))],
        compiler_params=pltpu.CompilerParams(has_side_effects=True),
    )(*tensors)


def _allreduce_small(slab):
    stages = 3

    def body(x_ref, o_ref, buf, send_sems, recv_sems):
        x, y, c = _position()
        peers = ((1 - x, y, c), (x, 1 - y, c), (x, y, 1 - c))
        o_ref[...] = x_ref[...]
        for k, peer in enumerate(peers):
            cp = pltpu.make_async_remote_copy(src_ref=o_ref, dst_ref=buf.at[k], send_sem=send_sems.at[k],
                                              recv_sem=recv_sems.at[k], device_id=peer, device_id_type=MESH)
            cp.start()
            cp.wait()
            o_ref[...] = o_ref[...] + buf[k]

    return pl.pallas_call(
        body, name="allreduce_small",
        in_specs=[pl.BlockSpec(memory_space=pltpu.VMEM)], out_specs=pl.BlockSpec(memory_space=pltpu.VMEM),
        out_shape=jax.ShapeDtypeStruct(slab.shape, slab.dtype),
        scratch_shapes=[pltpu.VMEM((stages,) + slab.shape, slab.dtype),
                        pltpu.SemaphoreType.DMA((stages,)), pltpu.SemaphoreType.DMA((stages,))],
        compiler_params=pltpu.CompilerParams(has_side_effects=True),
    )(slab)


def _row_tile(r):
    return min(r, 256)


def _sum4(stack, recv, me):
    _, r, c = stack.shape
    tr = _row_tile(r)

    def body(me_ref, own_ref, recv_ref, o_ref):
        o_ref[...] = (((own_ref[...] + recv_ref[0].astype(F32)) + recv_ref[1].astype(F32))
                      + recv_ref[2].astype(F32))

    return pl.pallas_call(
        body, name="sum_partials",
        grid_spec=pltpu.PrefetchScalarGridSpec(
            num_scalar_prefetch=1, grid=(r // tr,),
            in_specs=[pl.BlockSpec((None, tr, c), lambda i, me_ref: (me_ref[0], i, 0)),
                      pl.BlockSpec((len(CHIP_FLIPS), tr, c), lambda i, me_ref: (0, i, 0))],
            out_specs=pl.BlockSpec((tr, c), lambda i, me_ref: (i, 0))),
        out_shape=jax.ShapeDtypeStruct((r, c), F32), compiler_params=_params("parallel"),
    )(me, stack, recv)


def _adamw_math(w, g, m, v):
    m = ADAM_B1 * m + (1.0 - ADAM_B1) * g
    v = ADAM_B2 * v + (1.0 - ADAM_B2) * (g * g)
    m_hat = m / (1.0 - ADAM_B1 ** ADAM_STEP)
    v_hat = v / (1.0 - ADAM_B2 ** ADAM_STEP)
    delta = -ADAM_LR * (m_hat / (jnp.sqrt(v_hat) + ADAM_EPS) + ADAM_WD * w)
    return delta, m, v


def _adamw(w, m, v, g_parts, name):
    r, c = w.shape
    tr = _row_tile(r)
    n = len(g_parts)

    def body(*refs):
        w_ref, m_ref, v_ref = refs[:3]
        g_refs = refs[3:3 + n]
        g_out, d_out, m_out, v_out = refs[3 + n:]
        g = g_refs[0][...]
        for ref in g_refs[1:]:
            g = g + ref[...]
        g_out[...] = g
        d_out[...], m_out[...], v_out[...] = _adamw_math(w_ref[...], g, m_ref[...], v_ref[...])

    blk = pl.BlockSpec((tr, c), lambda i: (i, 0))
    return pl.pallas_call(
        body, name=name, grid=(r // tr,), in_specs=[blk] * (3 + n), out_specs=[blk] * 4,
        out_shape=[jax.ShapeDtypeStruct((r, c), F32)] * 4, compiler_params=_params("parallel"),
    )(w, m, v, *g_parts)


SHARDED = ("w_in", "w_lora_up", "a_lora_up", "w_out_a", "w_out_b", "w_out")
ROW_SHARDED = ("w_out",)
SMALL = ("norm_g", "shift_mu", "w0", "a0", "k_k", "k_a", "r_k", "lnx_w", "lnx_b", "f_bias", "q_norm_g", "k_norm_g",
         "final_norm_g")
WEIGHTS = ("norm_g", "w_in", "shift_mu", "w_lora_up", "w0", "a_lora_up", "a0", "k_k", "k_a", "r_k", "lnx_w", "lnx_b",
           "f_bias", "q_norm_g", "k_norm_g", "w_out_a", "w_out_b", "w_out", "final_norm_g")
SLAB_ROWS = 16
SLAB_COLS = SEC


def _to_slab(named, extra=None):
    rows = [jnp.pad(named[n].reshape(1, -1), ((0, 0), (0, SLAB_COLS - named[n].size))) for n in SMALL]
    if extra is not None:
        rows.append(jnp.pad(extra.reshape(1, -1), ((0, 0), (0, SLAB_COLS - extra.size))))
    rows.append(jnp.zeros((SLAB_ROWS - len(rows), SLAB_COLS), F32))
    return jnp.concatenate(rows, axis=0)


def _from_slab(slab, shapes):
    return {n: slab[i, :math.prod(shapes[n])].reshape(shapes[n]) for i, n in enumerate(SMALL)}


def _by_chip(g, name):
    if name in ROW_SHARDED:
        return g.reshape(N_CHIPS, g.shape[0] // N_CHIPS, g.shape[1])
    r, c = g.shape
    return g.reshape(r, N_CHIPS, c // N_CHIPS).transpose(1, 0, 2)


def _from_chips(stack, name):
    if name in ROW_SHARDED:
        return stack.reshape(-1, stack.shape[2])
    _, r, c = stack.shape
    return stack.transpose(1, 0, 2).reshape(r, N_CHIPS * c)


def kernel(x, norm_g, w_in, shift_mu, w_lora_up, w0, a_lora_up, a0, k_k, k_a, r_k, lnx_w, lnx_b, f_bias, q_norm_g, k_norm_g, w_out_a, w_out_b, w_out, final_norm_g, loss_target, m_norm_g, m_w_in, m_shift_mu, m_w_lora_up, m_w0, m_a_lora_up, m_a0, m_k_k, m_k_a, m_r_k, m_lnx_w, m_lnx_b, m_f_bias, m_q_norm_g, m_k_norm_g, m_w_out_a, m_w_out_b, m_w_out, m_final_norm_g, v_norm_g, v_w_in, v_shift_mu, v_w_lora_up, v_w0, v_a_lora_up, v_a0, v_k_k, v_k_a, v_r_k, v_lnx_w, v_lnx_b, v_f_bias, v_q_norm_g, v_k_norm_g, v_w_out_a, v_w_out_b, v_w_out, v_final_norm_g):
    w = dict(norm_g=norm_g, w_in=w_in, shift_mu=shift_mu, w_lora_up=w_lora_up, w0=w0, a_lora_up=a_lora_up, a0=a0,
             k_k=k_k, k_a=k_a, r_k=r_k, lnx_w=lnx_w, lnx_b=lnx_b, f_bias=f_bias, q_norm_g=q_norm_g,
             k_norm_g=k_norm_g, w_out_a=w_out_a, w_out_b=w_out_b, w_out=w_out, final_norm_g=final_norm_g)
    m = dict(norm_g=m_norm_g, w_in=m_w_in, shift_mu=m_shift_mu, w_lora_up=m_w_lora_up, w0=m_w0,
             a_lora_up=m_a_lora_up, a0=m_a0, k_k=m_k_k, k_a=m_k_a, r_k=m_r_k, lnx_w=m_lnx_w, lnx_b=m_lnx_b,
             f_bias=m_f_bias, q_norm_g=m_q_norm_g, k_norm_g=m_k_norm_g, w_out_a=m_w_out_a, w_out_b=m_w_out_b,
             w_out=m_w_out, final_norm_g=m_final_norm_g)
    v = dict(norm_g=v_norm_g, w_in=v_w_in, shift_mu=v_shift_mu, w_lora_up=v_w_lora_up, w0=v_w0,
             a_lora_up=v_a_lora_up, a0=v_a0, k_k=v_k_k, k_a=v_k_a, r_k=v_r_k, lnx_w=v_lnx_w, lnx_b=v_lnx_b,
             f_bias=v_f_bias, q_norm_g=v_q_norm_g, k_norm_g=v_k_norm_g, w_out_a=v_w_out_a, w_out_b=v_w_out_b,
             w_out=v_w_out, final_norm_g=v_final_norm_g)
    shapes = {n: w[n].shape for n in WEIGHTS}

    shard = {n: w[n][0].astype(BF16) for n in SHARDED}
    late = ("w_out_a", "w_out_b", "w_out")
    w01, up_stack, aup_stack = _gather_exchange(0, shard["w_in"], [shard["w_lora_up"], shard["a_lora_up"]]).run_alone(
        "gather_early")
    w_a = jnp.concatenate([w01[0], w01[1][:, :A_TAIL]], axis=1)

    def late_weights(arrived):
        w23 = arrived[0]
        w_b = jnp.concatenate([w01[1][:, A_TAIL:], w23[0][:, :B_TAIL], jnp.zeros((D_MODEL, SEC - FOX_REAL), BF16)], axis=1)
        w_g = jnp.concatenate([w23[0][:, B_TAIL:], w23[1]], axis=1)
        return (w_b, w_g, *[_from_chips(s, n) for n, s in zip(late, arrived[1:])])

    kept = {}

    def bwd_exchange(dw_b, dw_g, dwa, dwb, dwo):
        kept["w_in_1"] = jnp.stack([jnp.concatenate([dw_b[:, B_HEAD:FOX_REAL], dw_g[:, :G_HEAD]], axis=1), dw_g[:, G_HEAD:]])
        kept.update({n: _by_chip(g, n) for n, g in zip(late, (dwa, dwb, dwo))})
        return _scatter_exchange(1, kept["w_in_1"].astype(BF16), [kept[n].astype(BF16) for n in late])

    small = {n: w[n] for n in SMALL}
    loss_vec, grad_x, grads, sent = _device_grads(
        x[0], loss_target[0], small, w_a, _from_chips(up_stack, "w_lora_up"), _from_chips(aup_stack, "a_lora_up"),
        late_weights, _gather_exchange(1, shard["w_in"], [shard[n] for n in late]), bwd_exchange)

    total = _allreduce_small(_to_slab(grads, extra=loss_vec))
    loss = (0.5 / D_MODEL) * jnp.sum(total[len(SMALL)])
    slab_g, slab_d, slab_m, slab_v = _adamw(_to_slab(w), _to_slab(m), _to_slab(v), [total], "adamw_small")
    out_g, out_d, out_m, out_v = (_from_slab(s, shapes) for s in (total, slab_d, slab_m, slab_v))
    del slab_g

    dw_a, dw_b, _ = grads["w_in"]
    w_in_0 = jnp.stack([dw_a[:, :SHARD_COLS], jnp.concatenate([dw_a[:, SHARD_COLS:], dw_b[:, :B_HEAD]], axis=1)])
    loras = ("w_lora_up", "a_lora_up")
    kept.update({n: _by_chip(grads[n], n) for n in loras})
    sent_late = _scatter_exchange(0, w_in_0.astype(BF16), [kept[n].astype(BF16) for n in loras]).run_alone("scatter_late")
    xpos, ypos, _ = _position()
    y_index = ypos.astype(jnp.int32).reshape(1)
    me = (2 * xpos + ypos).astype(jnp.int32).reshape(1)
    core_sum = {"w_in": lax.cond(xpos == 1, lambda: _sum4(kept["w_in_1"], sent[0], y_index),
                                 lambda: _sum4(w_in_0, sent_late[0], y_index))}
    core_sum.update({n: _sum4(kept[n], r, me) for n, r in zip(late, sent[1:])})
    core_sum.update({n: _sum4(kept[n], r, me) for n, r in zip(loras, sent_late[1:])})
    sibling_sums = _swap_sibling([core_sum[n] for n in SHARDED])
    for n, theirs in zip(SHARDED, sibling_sums):
        g, d, m2, v2 = _adamw(w[n][0], m[n][0], v[n][0], [core_sum[n], theirs], "adamw_" + n)
        out_g[n], out_d[n], out_m[n], out_v[n] = (a.reshape(shapes[n]) for a in (g, d, m2, v2))

    return (loss, grad_x.reshape(x.shape), *[out_g[n] for n in WEIGHTS], *[out_d[n] for n in WEIGHTS],
            *[out_m[n] for n in WEIGHTS], *[out_v[n] for n in WEIGHTS])
```

```python
import functools
import math

import jax
import jax.numpy as jnp
from jax import lax
from jax.experimental import pallas as pl
from jax.experimental.pallas import tpu as pltpu

F32 = jnp.float32
BF16 = jnp.bfloat16

D_MODEL = 1024
D_HALF = 512
HEAD = 64
N_HEADS = 8
LORA = 64
RWKV_COLS = 2176
FOX_REAL = 2056
SEC = 2176
GATE_COLS = 2048
IN_COLS = 6280
N_CHIPS = 4
SHARD_COLS = IN_COLS // N_CHIPS
A_TAIL = RWKV_COLS - SHARD_COLS
B_HEAD = SHARD_COLS - A_TAIL
B_TAIL = FOX_REAL - B_HEAD
G_HEAD = SHARD_COLS - B_TAIL
RMS_EPS = 1e-6
LNX_EPS = 64e-5
ATT_SCALE = HEAD ** -0.5
NEG = -1e30

ADAM_LR = 0.001
ADAM_B1 = 0.9
ADAM_B2 = 0.999
ADAM_EPS = 1e-08
ADAM_WD = 0.01
ADAM_STEP = 10

LANES = 128
SUBLANES = 8
VMEM_LIMIT = 56 * 1024 * 1024
MESH = pl.DeviceIdType.MESH


def _params(*sem):
    return pltpu.CompilerParams(dimension_semantics=sem if sem else None, vmem_limit_bytes=VMEM_LIMIT)


def _sigmoid(x):
    return 1.0 / (1.0 + jnp.exp(-x))


def _log_sigmoid(x):
    return jnp.minimum(x, 0.0) - jnp.log(1.0 + jnp.exp(-jnp.abs(x)))


def _head_ones():
    r = lax.broadcasted_iota(jnp.int32, (LANES, LANES), 0) >> 6
    c = lax.broadcasted_iota(jnp.int32, (LANES, LANES), 1) >> 6
    return (r == c).astype(BF16)


def _split3(x):
    hi = x.astype(BF16)
    r1 = x - hi.astype(F32)
    mid = r1.astype(BF16)
    lo = (r1 - mid.astype(F32)).astype(BF16)
    return hi, mid, lo


def _exact_dot(x, ones_bf16, ones_first=False):
    out = None
    for piece in _split3(x):
        if ones_first:
            t = jnp.dot(ones_bf16, piece, preferred_element_type=F32)
        else:
            t = jnp.dot(piece, ones_bf16, preferred_element_type=F32)
        out = t if out is None else out + t
    return out


def _head_sum(x, bd):
    n = x.shape[1] // LANES
    parts = [_exact_dot(x[:, i * LANES:(i + 1) * LANES], bd) for i in range(n)]
    return parts[0] if n == 1 else jnp.concatenate(parts, axis=1)


def _dot_nt(a, b):
    return lax.dot_general(a, b, (((1,), (1,)), ((), ())), preferred_element_type=F32)


def _dot_tn(a, b):
    return lax.dot_general(a, b, (((0,), (0,)), ((), ())), preferred_element_type=F32)


def _colsum(x):
    return jnp.sum(x, axis=0, keepdims=True)


def _rmsnorm_in(x, g, tm=512):
    s, d = x.shape

    def body(x_ref, g_ref, h_ref):
        xv = x_ref[...]
        r = lax.rsqrt(jnp.mean(xv * xv, axis=-1, keepdims=True) + RMS_EPS)
        h_ref[...] = (xv * r * g_ref[...]).astype(BF16)

    return pl.pallas_call(
        body, name="rmsnorm_in", grid=(s // tm,),
        in_specs=[pl.BlockSpec((tm, d), lambda i: (i, 0)), pl.BlockSpec((1, d), lambda i: (0, 0))],
        out_specs=pl.BlockSpec((tm, d), lambda i: (i, 0)),
        out_shape=jax.ShapeDtypeStruct((s, d), BF16), compiler_params=_params("parallel"),
    )(x, g)


def _matmul_nn(a, b, name, tm=512):
    m, k = a.shape
    n = b.shape[1]

    def body(a_ref, b_ref, o_ref):
        o_ref[...] = jnp.dot(a_ref[...], b_ref[...], preferred_element_type=F32)

    return pl.pallas_call(
        body, name=name, grid=(m // tm,),
        in_specs=[pl.BlockSpec((tm, k), lambda i: (i, 0)), pl.BlockSpec((k, n), lambda i: (0, 0))],
        out_specs=pl.BlockSpec((tm, n), lambda i: (i, 0)),
        out_shape=jax.ShapeDtypeStruct((m, n), F32), compiler_params=_params("parallel"),
    )(a, b)


def _matmul_tn_acc(at, b, name, tk=512):
    m, k = at.shape
    n = b.shape[1]

    def body(a_ref, b_ref, o_ref):
        j = pl.program_id(0)

        @pl.when(j == 0)
        def _():
            o_ref[...] = jnp.zeros_like(o_ref)

        o_ref[...] += jnp.dot(a_ref[...], b_ref[...].astype(BF16), preferred_element_type=F32)

    return pl.pallas_call(
        body, name=name, grid=(k // tk,),
        in_specs=[pl.BlockSpec((m, tk), lambda j: (0, j)), pl.BlockSpec((tk, n), lambda j: (j, 0))],
        out_specs=pl.BlockSpec((m, n), lambda j: (0, 0)),
        out_shape=jax.ShapeDtypeStruct((m, n), F32), compiler_params=_params("arbitrary"),
    )(at, b)


def _inproj_bwd(du_a, du_b, du_g, w_a, w_b, w_g, x, dx2, g, tm=256):
    s, d = x.shape

    def body(da_ref, db_ref, dg_ref, wa_ref, wb_ref, wg_ref, x_ref, dx2_ref, g_ref, gx_ref, gg_ref):
        i = pl.program_id(0)

        @pl.when(i == 0)
        def _():
            gg_ref[...] = jnp.zeros_like(gg_ref)

        dh = _dot_nt(da_ref[...].astype(BF16), wa_ref[...])
        dh += _dot_nt(db_ref[...].astype(BF16), wb_ref[...])
        dh += _dot_nt(dg_ref[...].astype(BF16), wg_ref[...])
        xv = x_ref[...]
        r = lax.rsqrt(jnp.mean(xv * xv, axis=-1, keepdims=True) + RMS_EPS)
        xh = xv * r
        gg_ref[...] += _colsum(dh * xh)
        dxh = dh * g_ref[...]
        gx_ref[...] = dx2_ref[...] + r * (dxh - xh * jnp.mean(dxh * xh, axis=-1, keepdims=True))

    row = lambda w: pl.BlockSpec((tm, w), lambda i: (i, 0))
    full = lambda a: pl.BlockSpec(a.shape, lambda i: (0, 0))
    return pl.pallas_call(
        body, name="inproj_bwd", grid=(s // tm,),
        in_specs=[row(SEC), row(SEC), row(GATE_COLS), full(w_a), full(w_b), full(w_g), row(d), row(d), full(g)],
        out_specs=[row(d), pl.BlockSpec((1, d), lambda i: (0, 0))],
        out_shape=[jax.ShapeDtypeStruct((s, d), F32), jax.ShapeDtypeStruct((1, d), F32)],
        compiler_params=_params("arbitrary"),
    )(du_a, du_b, du_g, w_a, w_b, w_g, x, dx2, g)


def _rwkv_elementwise(ua, prev_row, first, mu, wl, w0, a0, kkw, kaw, bd):
    tm = ua.shape[0]
    rows = lax.broadcasted_iota(jnp.int32, (tm, 1), 0)
    prev = jnp.where(first, jnp.zeros_like(prev_row), prev_row)
    shifted = jnp.where(rows == 0, prev, pltpu.roll(ua, 1, 0))
    delta = shifted - ua
    us = ua + delta * mu
    r = us[:, 0:512]
    k0 = us[:, 512:1024]
    v = us[:, 1024:1536]
    lo = us[:, 1536:1664]
    gate = us[:, 1664:2176]
    lane = lax.broadcasted_iota(jnp.int32, (1, LANES), 1)
    th = jnp.tanh(lo)
    lin = jnp.where(lane < LORA, th, lo)
    ll = jnp.dot(lin.astype(BF16), wl, preferred_element_type=F32)
    sz = _sigmoid(w0 + ll[:, :512])
    e = sz * math.exp(-0.5)
    dec = jnp.exp(-e)
    a = _sigmoid(a0 + ll[:, 512:])
    kk0 = k0 * kkw
    ss = _head_sum(kk0 * kk0, bd)
    nrm = jnp.maximum(jnp.sqrt(ss), 1e-12)
    kk = kk0 / nrm
    k = k0 * (1.0 + (a - 1.0) * kaw)
    return dict(delta=delta, us=us, r=r, k0=k0, v=v, lo=lo, gate=gate, th=th, lin=lin, sz=sz, e=e, dec=dec,
                a=a, kk0=kk0, ss=ss, nrm=nrm, kk=kk, k=k)


def _rwkv_prep(u_a, mu, wl, w0, a0, kkw, kaw, tm=256):
    s = u_a.shape[0]

    def body(ua_ref, prev_ref, mu_ref, wl_ref, w0_ref, a0_ref, kkw_ref, kaw_ref,
             r_ref, w_ref, k_ref, v_ref, a_ref, b_ref, g_ref):
        i = pl.program_id(0)
        f = _rwkv_elementwise(ua_ref[...], prev_ref[7:8, :], i == 0, mu_ref[...], wl_ref[...], w0_ref[...],
                              a0_ref[...], kkw_ref[...], kaw_ref[...], _head_ones())
        r_ref[...] = f["r"]
        w_ref[...] = f["dec"]
        k_ref[...] = f["k"]
        v_ref[...] = f["v"]
        a_ref[...] = -f["kk"]
        b_ref[...] = f["kk"] * f["a"]
        g_ref[...] = f["gate"]

    vec = lambda w: pl.BlockSpec((1, w), lambda i: (0, 0))
    out = pl.BlockSpec((tm, D_HALF), lambda i: (i, 0))
    return pl.pallas_call(
        body, name="rwkv_prep", grid=(s // tm,),
        in_specs=[pl.BlockSpec((tm, SEC), lambda i: (i, 0)),
                  pl.BlockSpec((8, SEC), lambda i: (jnp.maximum(i * (tm // 8) - 1, 0), 0)),
                  vec(SEC), pl.BlockSpec((LANES, 2 * D_HALF), lambda i: (0, 0)),
                  vec(D_HALF), vec(D_HALF), vec(D_HALF), vec(D_HALF)],
        out_specs=[out] * 7,
        out_shape=[jax.ShapeDtypeStruct((s, D_HALF), F32)] * 7,
        compiler_params=_params("parallel"),
    )(u_a, u_a, mu, wl, w0, a0, kkw, kaw)


SCAN_TB = 128
N_PAIRS = 4


def _pair_sum(x, left):
    s_l = jnp.sum(jnp.where(left, x, 0.0), axis=1, keepdims=True)
    s_r = jnp.sum(jnp.where(left, 0.0, x), axis=1, keepdims=True)
    return jnp.where(left, s_l, s_r)


def _quad_consts():
    lane = lax.broadcasted_iota(jnp.int32, (HEAD, 2 * LANES), 1)
    rowi = lax.broadcasted_iota(jnp.int32, (HEAD, 2 * LANES), 0)
    diag2 = rowi == (lane & (HEAD - 1))
    r = lax.broadcasted_iota(jnp.int32, (2 * LANES, 2 * LANES), 0) >> 6
    c = lax.broadcasted_iota(jnp.int32, (2 * LANES, 2 * LANES), 1) >> 6
    return diag2, (r == c).astype(BF16)


def _rows_to_columns(x8, diag2, bd2):
    hi = x8.astype(BF16).astype(F32)
    lo = x8 - hi
    lhs = jnp.concatenate([jnp.where(diag2, piece[i:i + 1], 0.0).astype(BF16)
                           for piece in (hi, lo) for i in range(SUBLANES)], axis=0)
    res = jnp.dot(lhs, bd2, preferred_element_type=F32)
    half = SUBLANES * HEAD
    return res[:half] + res[half:]


def _diag_rows(qtile, diag2, bd2, sub_row2):
    half = SUBLANES * HEAD
    res = jnp.dot(qtile, bd2, preferred_element_type=F32)
    out = jnp.zeros((SUBLANES, 2 * LANES), F32)
    for i in range(SUBLANES):
        t = res[i * HEAD:(i + 1) * HEAD] + res[half + i * HEAD:half + (i + 1) * HEAD]
        out = jnp.where(sub_row2 == i, _colsum(jnp.where(diag2, t, 0.0)), out)
    return out


def _store_pieces(qbuf, slot, p, i, x):
    half = SUBLANES * HEAD
    inner = slice((p % 2) * LANES, (p % 2 + 1) * LANES)
    hi = x.astype(BF16)
    qbuf[slot, p // 2, i * HEAD:(i + 1) * HEAD, inner] = hi
    qbuf[slot, p // 2, half + i * HEAD:half + (i + 1) * HEAD, inner] = (x - hi.astype(F32)).astype(BF16)


def _left_half():
    return lax.broadcasted_iota(jnp.int32, (HEAD, LANES), 1) < HEAD


def _split_refs(refs, n_rows, n_out, exchange):
    n_in = len(exchange.operands) if exchange else 0
    n_ex_out = len(exchange.out_shapes) if exchange else 0
    refs = list(refs)
    rows, refs = refs[:n_rows], refs[n_rows:]
    ex_in, refs = refs[:n_in], refs[n_in:]
    outs, refs = refs[:n_out], refs[n_out:]
    ex_out, refs = refs[:n_ex_out], refs[n_ex_out:]
    scratch, sems = (refs[:-3], refs[-3:]) if exchange else (refs, None)
    moves = exchange.moves(ex_in, ex_out, sems) if exchange else None
    return rows, outs, scratch, moves


def _wkv_fwd(r, w, k, a, b, v, exchange=None):
    s = r.shape[0]
    tb = SCAN_TB
    nb = s // tb

    def body(*refs):
        (r_ref, w_ref, k_ref, a_ref, b_ref, v_ref), (y_ref, st_ref), (state, vbuf, qbuf), moves = _split_refs(
            refs, 6, 2, exchange)
        g = pl.program_id(0)
        if moves:
            moves.start(also=(g == 0))

        @pl.when(g == 0)
        def _():
            state[...] = jnp.zeros_like(state)
            qbuf[...] = jnp.zeros_like(qbuf)

        left = _left_half()
        diag2, bd2 = _quad_consts()
        sub_row2 = lax.broadcasted_iota(jnp.int32, (SUBLANES, 2 * LANES), 0)
        groups = tb // SUBLANES
        quads = [slice(g2 * 2 * LANES, (g2 + 1) * 2 * LANES) for g2 in range(2)]

        def rows_of(q):
            return pl.ds(pl.multiple_of(q * SUBLANES, SUBLANES), SUBLANES)

        def v_tiles(q, slot):
            v8 = v_ref[rows_of(q), :]
            for g2 in range(2):
                vbuf[slot, g2] = _rows_to_columns(v8[:, quads[g2]], diag2, bd2)

        def chain(q, slot):
            rows8 = rows_of(q)
            a8, w8, b8, k8, r8 = (x[rows8, :] for x in (a_ref, w_ref, b_ref, k_ref, r_ref))
            sp = [state[p] for p in range(N_PAIRS)]
            for i in range(SUBLANES):
                row = slice(i, i + 1)
                sa = [_pair_sum(sp[p] * a8[row, p * LANES:(p + 1) * LANES], left) for p in range(N_PAIRS)]
                for p in range(N_PAIRS):
                    lanes = slice(p * LANES, (p + 1) * LANES)
                    vt = vbuf[slot, p // 2, i * HEAD:(i + 1) * HEAD, (p % 2) * LANES:(p % 2 + 1) * LANES]
                    sp[p] = sp[p] * w8[row, lanes] + sa[p] * b8[row, lanes] + vt * k8[row, lanes]
                    st_ref[q * SUBLANES + i, p] = sp[p]
                    _store_pieces(qbuf, slot, p, i, sp[p] * r8[row, lanes])
            for p in range(N_PAIRS):
                state[p] = sp[p]

        def y_rows(q, slot):
            for g2 in range(2):
                y_ref[rows_of(q), quads[g2]] = _diag_rows(qbuf[slot, g2], diag2, bd2, sub_row2)

        v_tiles(0, 0)

        def two_groups(j, carry):
            q0 = 2 * j
            v_tiles(q0 + 1, 1)
            chain(q0, 0)
            y_rows(jnp.maximum(q0 - 1, 0), 1)
            v_tiles(jnp.minimum(q0 + 2, groups - 1), 0)
            chain(q0 + 1, 1)
            y_rows(q0, 0)
            return carry

        lax.fori_loop(0, groups // 2, two_groups, 0)
        y_rows(groups - 1, 1)
        if moves:
            moves.wait(also=(g == nb - 1))

    rows = pl.BlockSpec((tb, D_HALF), lambda g: (g, 0))
    ex_in = exchange.operands if exchange else []
    ex_out = exchange.out_shapes if exchange else []
    res = pl.pallas_call(
        body, name="wkv_fwd", grid=(nb,),
        in_specs=[rows] * 6 + [ANY] * len(ex_in),
        out_specs=[rows, pl.BlockSpec((tb, N_PAIRS, HEAD, LANES), lambda g: (g, 0, 0, 0))] + [ANY] * len(ex_out),
        out_shape=[jax.ShapeDtypeStruct((s, D_HALF), F32),
                   jax.ShapeDtypeStruct((s, N_PAIRS, HEAD, LANES), F32)] + ex_out,
        scratch_shapes=[pltpu.VMEM((N_PAIRS, HEAD, LANES), F32),
                        pltpu.VMEM((2, 2, SUBLANES * HEAD, 2 * LANES), F32),
                        pltpu.VMEM((2, 2, 2 * SUBLANES * HEAD, 2 * LANES), BF16)]
                       + (exchange.scratch() if exchange else []),
        compiler_params=_params("arbitrary"),
    )(r, w, k, a, b, v, *ex_in)
    return res[0], res[1], list(res[2:])


def _wkv_bwd(r, w, k, a, b, v, dy, st, exchange=None):
    s = r.shape[0]
    tb = SCAN_TB
    nb = s // tb

    def body(*refs):
        ((r_ref, w_ref, k_ref, a_ref, b_ref, v_ref, dy_ref, st_ref, before_ref),
         (dr_ref, dw_ref, dk_ref, dv_ref, da_ref, db_ref), (dstate, vbuf, qbuf), moves) = _split_refs(refs, 9, 6, exchange)
        g = pl.program_id(0)
        first_block = g == nb - 1
        if moves:
            moves.start(also=(g == 0))

        @pl.when(g == 0)
        def _():
            dstate[...] = jnp.zeros_like(dstate)
            qbuf[...] = jnp.zeros_like(qbuf)

        left = _left_half()
        diag2, bd2 = _quad_consts()
        sub_row = lax.broadcasted_iota(jnp.int32, (SUBLANES, LANES), 0)
        sub_row2 = lax.broadcasted_iota(jnp.int32, (SUBLANES, 2 * LANES), 0)
        groups = tb // SUBLANES
        quads = [slice(g2 * 2 * LANES, (g2 + 1) * 2 * LANES) for g2 in range(2)]
        row_refs = (dr_ref, dw_ref, dk_ref, da_ref, db_ref)

        def rows_of(q):
            return pl.ds(pl.multiple_of(q * SUBLANES, SUBLANES), SUBLANES)

        def column_tiles(q, slot):
            rows8 = rows_of(q)
            for kind, ref in enumerate((v_ref, dy_ref)):
                x8 = ref[rows8, :]
                for g2 in range(2):
                    vbuf[slot, kind, g2] = _rows_to_columns(x8[:, quads[g2]], diag2, bd2)

        def chain(q, slot):
            rows8 = rows_of(q)
            a8, w8, b8, k8, r8 = (x[rows8, :] for x in (a_ref, w_ref, b_ref, k_ref, r_ref))
            dsp = [dstate[p] for p in range(N_PAIRS)]
            outs = [[jnp.zeros((SUBLANES, LANES), F32) for _ in row_refs] for _ in range(N_PAIRS)]
            after = [st_ref[q * SUBLANES + SUBLANES - 1, p] for p in range(N_PAIRS)]
            for i in reversed(range(SUBLANES)):
                row = slice(i, i + 1)
                pl_ = [slice(p * LANES, (p + 1) * LANES) for p in range(N_PAIRS)]
                tile = [(p // 2, slice(i * HEAD, (i + 1) * HEAD), slice((p % 2) * LANES, (p % 2 + 1) * LANES))
                        for p in range(N_PAIRS)]
                if i > 0:
                    sp = [st_ref[q * SUBLANES + i - 1, p] for p in range(N_PAIRS)]
                else:
                    sp = [jnp.where(q == 0, jnp.where(first_block, 0.0, before_ref[0, p]),
                                    st_ref[jnp.maximum(q * SUBLANES - 1, 0), p]) for p in range(N_PAIRS)]
                dyt = [vbuf[(slot, 1) + tile[p]] for p in range(N_PAIRS)]
                ds = [dsp[p] + dyt[p] * r8[row, pl_[p]] for p in range(N_PAIRS)]
                dsa = [_pair_sum(ds[p] * b8[row, pl_[p]], left) for p in range(N_PAIRS)]
                sa = [_pair_sum(sp[p] * a8[row, pl_[p]], left) for p in range(N_PAIRS)]
                for p in range(N_PAIRS):
                    ar, wr, br, kr = (x[row, pl_[p]] for x in (a8, w8, b8, k8))
                    vt = vbuf[(slot, 0) + tile[p]]
                    dsp[p] = ds[p] * wr + dsa[p] * ar
                    new = (_colsum(after[p] * dyt[p]), _colsum(ds[p] * sp[p]), _colsum(ds[p] * vt),
                           _colsum(sp[p] * dsa[p]), _colsum(ds[p] * sa[p]))
                    outs[p] = [jnp.where(sub_row == i, n, o) for n, o in zip(new, outs[p])]
                    _store_pieces(qbuf, slot, p, i, ds[p] * kr)
                after = sp
            for p in range(N_PAIRS):
                dstate[p] = dsp[p]
                for ref, o in zip(row_refs, outs[p]):
                    ref[rows8, p * LANES:(p + 1) * LANES] = o

        def dv_rows(q, slot):
            for g2 in range(2):
                dv_ref[rows_of(q), quads[g2]] = _diag_rows(qbuf[slot, g2], diag2, bd2, sub_row2)

        column_tiles(groups - 1, 0)

        def two_groups(j, carry):
            q0 = groups - 1 - 2 * j
            column_tiles(q0 - 1, 1)
            chain(q0, 0)
            dv_rows(jnp.minimum(q0 + 1, groups - 1), 1)
            column_tiles(jnp.maximum(q0 - 2, 0), 0)
            chain(q0 - 1, 1)
            dv_rows(q0, 0)
            return carry

        lax.fori_loop(0, groups // 2, two_groups, 0)
        dv_rows(0, 1)
        if moves:
            moves.wait(also=(g == nb - 1))

    rows = pl.BlockSpec((tb, D_HALF), lambda g: (nb - 1 - g, 0))
    ex_in = exchange.operands if exchange else []
    ex_out = exchange.out_shapes if exchange else []
    res = pl.pallas_call(
        body, name="wkv_bwd", grid=(nb,),
        in_specs=[rows] * 7 + [pl.BlockSpec((tb, N_PAIRS, HEAD, LANES), lambda g: (nb - 1 - g, 0, 0, 0)),
                               pl.BlockSpec((1, N_PAIRS, HEAD, LANES),
                                            lambda g: (jnp.maximum((nb - 1 - g) * tb - 1, 0), 0, 0, 0))]
                 + [ANY] * len(ex_in),
        out_specs=[rows] * 6 + [ANY] * len(ex_out),
        out_shape=[jax.ShapeDtypeStruct((s, D_HALF), F32)] * 6 + ex_out,
        scratch_shapes=[pltpu.VMEM((N_PAIRS, HEAD, LANES), F32),
                        pltpu.VMEM((2, 2, 2, SUBLANES * HEAD, 2 * LANES), F32),
                        pltpu.VMEM((2, 2, 2 * SUBLANES * HEAD, 2 * LANES), BF16)]
                       + (exchange.scratch() if exchange else []),
        compiler_params=_params("arbitrary"),
    )(r, w, k, a, b, v, dy, st, st, *ex_in)
    return list(res[:6]), list(res[6:])


def _rwkv_post_math(y, r, k, v, gate, lw, lb, rk, bd):
    mean = _head_sum(y, bd) * (1.0 / HEAD)
    yc = y - mean
    var = _head_sum(yc * yc, bd) * (1.0 / HEAD)
    rstd = lax.rsqrt(var + LNX_EPS)
    yn = yc * rstd
    rkk = _head_sum(r * k * rk, bd)
    sg = _sigmoid(gate)
    pre = yn * lw + lb + rkk * v
    return yn, rstd, rkk, sg, pre


def _rwkv_post(y, r, k, v, gate, lw, lb, rk, tm=256):
    s = y.shape[0]

    def body(y_ref, r_ref, k_ref, v_ref, g_ref, lw_ref, lb_ref, rk_ref, o_ref):
        gate_v = g_ref[...]
        _, _, _, sg, pre = _rwkv_post_math(y_ref[...], r_ref[...], k_ref[...], v_ref[...], gate_v,
                                           lw_ref[...], lb_ref[...], rk_ref[...], _head_ones())
        o_ref[...] = pre * (gate_v * sg)

    blk = pl.BlockSpec((tm, D_HALF), lambda i: (i, 0))
    vec = pl.BlockSpec((1, D_HALF), lambda i: (0, 0))
    return pl.pallas_call(
        body, name="rwkv_post", grid=(s // tm,),
        in_specs=[blk] * 5 + [vec] * 3, out_specs=blk,
        out_shape=jax.ShapeDtypeStruct((s, D_HALF), F32), compiler_params=_params("parallel"),
    )(y, r, k, v, gate, lw, lb, rk)


def _rwkv_post_bwd(dmix, y, r, k, v, gate, lw, lb, rk, tm=256):
    s = y.shape[0]

    def body(dm_ref, y_ref, r_ref, k_ref, v_ref, g_ref, lw_ref, lb_ref, rk_ref,
             dy_ref, dr_ref, dk_ref, dv_ref, dg_ref, dlw_ref, dlb_ref, drk_ref):
        i = pl.program_id(0)

        @pl.when(i == 0)
        def _():
            dlw_ref[...] = jnp.zeros_like(dlw_ref)
            dlb_ref[...] = jnp.zeros_like(dlb_ref)
            drk_ref[...] = jnp.zeros_like(drk_ref)

        bd = _head_ones()
        rv, kv, vv, gate_v, lw_v, rk_v = r_ref[...], k_ref[...], v_ref[...], g_ref[...], lw_ref[...], rk_ref[...]
        yn, rstd, rkk, sg, pre = _rwkv_post_math(y_ref[...], rv, kv, vv, gate_v, lw_v, lb_ref[...], rk_v, bd)
        dm = dm_ref[...]
        dg_ref[...] = dm * pre * (sg * (1.0 + gate_v * (1.0 - sg)))
        dpre = dm * (gate_v * sg)
        dlw_ref[...] += _colsum(dpre * yn)
        dlb_ref[...] += _colsum(dpre)
        dyn = dpre * lw_v
        m1 = _head_sum(dyn, bd) * (1.0 / HEAD)
        m2 = _head_sum(dyn * yn, bd) * (1.0 / HEAD)
        dy_ref[...] = rstd * (dyn - m1 - yn * m2)
        dv_ref[...] = dpre * rkk
        drkk = _head_sum(dpre * vv, bd)
        dr_ref[...] = drkk * kv * rk_v
        dk_ref[...] = drkk * rv * rk_v
        drk_ref[...] += _colsum(drkk * rv * kv)

    blk = pl.BlockSpec((tm, D_HALF), lambda i: (i, 0))
    vec = pl.BlockSpec((1, D_HALF), lambda i: (0, 0))
    return pl.pallas_call(
        body, name="rwkv_post_bwd", grid=(s // tm,),
        in_specs=[blk] * 6 + [vec] * 3, out_specs=[blk] * 5 + [vec] * 3,
        out_shape=[jax.ShapeDtypeStruct((s, D_HALF), F32)] * 5 + [jax.ShapeDtypeStruct((1, D_HALF), F32)] * 3,
        compiler_params=_params("arbitrary"),
    )(dmix, y, r, k, v, gate, lw, lb, rk)


def _rwkv_prep_bwd(u_a, grads, mu, wl, w0, a0, kkw, kaw, tm=256):
    s = u_a.shape[0]
    nb = s // tm

    def body(ua_ref, prev_ref, drs_ref, dws_ref, dks_ref, dvs_ref, das_ref, dbs_ref, drb_ref, dkb_ref, dvb_ref,
             dgt_ref, mu_ref, wl_ref, w0_ref, a0_ref, kkw_ref, kaw_ref,
             du_ref, dmu_ref, dwl_ref, dw0_ref, da0_ref, dkkw_ref, dkaw_ref, carry):
        i = pl.program_id(0)

        @pl.when(i == 0)
        def _():
            carry[...] = jnp.zeros_like(carry)
            for ref in (dmu_ref, dwl_ref, dw0_ref, da0_ref, dkkw_ref, dkaw_ref):
                ref[...] = jnp.zeros_like(ref)

        bd = _head_ones()
        mu_v, wl_v, kkw_v, kaw_v = mu_ref[...], wl_ref[...], kkw_ref[...], kaw_ref[...]
        f = _rwkv_elementwise(ua_ref[...], prev_ref[7:8, :], i == nb - 1, mu_v, wl_v, w0_ref[...],
                              a0_ref[...], kkw_v, kaw_v, bd)
        a, kk, k0 = f["a"], f["kk"], f["k0"]
        dk = dks_ref[...] + dkb_ref[...]
        dbs = dbs_ref[...]
        dkk = dbs * a - das_ref[...]
        da = dbs * kk + dk * k0 * kaw_v
        dk0 = dk * (1.0 + (a - 1.0) * kaw_v)
        dkaw_ref[...] += _colsum(dk * k0 * (a - 1.0))
        inv = 1.0 / f["nrm"]
        proj = _head_sum(dkk * kk, bd)
        dkk0 = jnp.where(f["ss"] > 1e-24, (dkk - kk * proj) * inv, dkk * inv)
        dk0 = dk0 + dkk0 * kkw_v
        dkkw_ref[...] += _colsum(dkk0 * k0)
        dza = da * a * (1.0 - a)
        da0_ref[...] += _colsum(dza)
        dz = -dws_ref[...] * f["dec"] * f["e"] * (1.0 - f["sz"])
        dw0_ref[...] += _colsum(dz)
        dll = jnp.concatenate([dz, dza], axis=1).astype(BF16)
        dwl_ref[...] += _dot_tn(f["lin"].astype(BF16), dll)
        dlin = _dot_nt(dll, wl_v)
        lane = lax.broadcasted_iota(jnp.int32, (1, LANES), 1)
        th = f["th"]
        dlo = jnp.where(lane < LORA, dlin * (1.0 - th * th), dlin)
        dus = jnp.concatenate([drs_ref[...] + drb_ref[...], dk0, dvs_ref[...] + dvb_ref[...], dlo, dgt_ref[...]],
                              axis=1)
        dmu_ref[...] += _colsum(dus * f["delta"])
        g1 = dus * mu_v
        rows = lax.broadcasted_iota(jnp.int32, (tm, 1), 0)
        up = jnp.where(rows == tm - 1, carry[...], pltpu.roll(g1, tm - 1, 0))
        du_ref[...] = dus - g1 + up
        carry[...] = g1[0:1, :]

    rev = lambda w: pl.BlockSpec((tm, w), lambda i: (nb - 1 - i, 0))
    vec = lambda w: pl.BlockSpec((1, w), lambda i: (0, 0))
    wl_spec = pl.BlockSpec((LANES, 2 * D_HALF), lambda i: (0, 0))
    return pl.pallas_call(
        body, name="rwkv_prep_bwd", grid=(nb,),
        in_specs=[rev(SEC), pl.BlockSpec((8, SEC), lambda i: (jnp.maximum((nb - 1 - i) * (tm // 8) - 1, 0), 0))]
                 + [rev(D_HALF)] * 10 + [vec(SEC), wl_spec] + [vec(D_HALF)] * 4,
        out_specs=[rev(SEC), vec(SEC), wl_spec] + [vec(D_HALF)] * 4,
        out_shape=[jax.ShapeDtypeStruct((s, SEC), F32), jax.ShapeDtypeStruct((1, SEC), F32),
                   jax.ShapeDtypeStruct((LANES, 2 * D_HALF), F32)] + [jax.ShapeDtypeStruct((1, D_HALF), F32)] * 4,
        scratch_shapes=[pltpu.VMEM((1, SEC), F32)],
        compiler_params=_params("arbitrary"),
    )(u_a, u_a, *grads, mu, wl, w0, a0, kkw, kaw)


def _tri(tm, lower):
    r = lax.broadcasted_iota(jnp.int32, (tm, tm), 0)
    c = lax.broadcasted_iota(jnp.int32, (tm, tm), 1)
    return ((r >= c) if lower else (r <= c)).astype(BF16)


def _head_rms(x, g, bd):
    rinv = lax.rsqrt(_head_sum(x * x, bd) * (1.0 / HEAD) + RMS_EPS)
    xh = x * rinv
    return xh, rinv, xh * g


def _fox_prep(u_b, fb, qg, kg, tm=256):
    s = u_b.shape[0]

    def body(ub_ref, fb_ref, qg_ref, kg_ref, q_ref, k_ref, v_ref, cc_ref, cr_ref, carry):
        i = pl.program_id(0)

        @pl.when(i == 0)
        def _():
            carry[...] = jnp.zeros_like(carry)

        bd = _head_ones()
        _, _, qn = _head_rms(ub_ref[:, 0:512], qg_ref[...], bd)
        _, _, kn = _head_rms(ub_ref[:, 512:1024], kg_ref[...], bd)
        q_ref[...] = (qn * ATT_SCALE).astype(BF16)
        k_ref[...] = kn.astype(BF16)
        v_ref[...] = ub_ref[:, 1024:1536].astype(BF16)
        lane = lax.broadcasted_iota(jnp.int32, (1, LANES), 1)
        logf = jnp.where(lane < N_HEADS, _log_sigmoid(ub_ref[:, 2048:2176] + fb_ref[...]), 0.0)
        cum = _exact_dot(logf, _tri(tm, True), ones_first=True) + carry[...]
        for h in range(N_HEADS):
            cc_ref[h] = jnp.broadcast_to(cum[:, h:h + 1], (tm, LANES))
        cr_ref[...] = jnp.transpose(cum)[0:N_HEADS, :]
        carry[...] = cum[tm - 1:tm, :]

    blk = pl.BlockSpec((tm, D_HALF), lambda i: (i, 0))
    return pl.pallas_call(
        body, name="fox_prep", grid=(s // tm,),
        in_specs=[pl.BlockSpec((tm, SEC), lambda i: (i, 0)), pl.BlockSpec((1, LANES), lambda i: (0, 0)),
                  pl.BlockSpec((1, D_HALF), lambda i: (0, 0)), pl.BlockSpec((1, D_HALF), lambda i: (0, 0))],
        out_specs=[blk, blk, blk, pl.BlockSpec((N_HEADS, tm, LANES), lambda i: (0, i, 0)),
                   pl.BlockSpec((N_HEADS, tm), lambda i: (0, i))],
        out_shape=[jax.ShapeDtypeStruct((s, D_HALF), BF16)] * 3
                  + [jax.ShapeDtypeStruct((N_HEADS, s, LANES), F32), jax.ShapeDtypeStruct((N_HEADS, s), F32)],
        scratch_shapes=[pltpu.VMEM((1, LANES), F32)],
        compiler_params=_params("arbitrary"),
    )(u_b, fb, qg, kg)


ATT_T = 256


def _attn_fwd(q, k, v, cc, cr, u_b):
    s = q.shape[0]
    t = ATT_T
    nblk = s // t

    def body(q_ref, k_ref, v_ref, cc_ref, cr_ref, g_ref, o_ref, mix_ref, lse_ref, m_sc, l_sc, acc_sc):
        i = pl.program_id(0)
        j = pl.program_id(1)

        @pl.when(j == 0)
        def _():
            m_sc[...] = jnp.full_like(m_sc, NEG)
            l_sc[...] = jnp.zeros_like(l_sc)
            acc_sc[...] = jnp.zeros_like(acc_sc)

        @pl.when(j <= i)
        def _():
            row = i * t + lax.broadcasted_iota(jnp.int32, (t, t), 0)
            col = j * t + lax.broadcasted_iota(jnp.int32, (t, t), 1)
            causal = row >= col
            left = lax.broadcasted_iota(jnp.int32, (1, LANES), 1) < HEAD
            for p in range(N_PAIRS):
                lanes = slice(p * LANES, (p + 1) * LANES)
                q2, k2, v2 = q_ref[:, lanes], k_ref[:, lanes], v_ref[:, lanes]
                acc2 = acc_sc[:, lanes]
                for e in range(2):
                    h = 2 * p + e
                    msk = left if e == 0 else jnp.logical_not(left)
                    sc = _dot_nt(jnp.where(msk, q2, jnp.zeros_like(q2)), k2)
                    sc = sc + (_wide(cc_ref[h]) - cr_ref[h:h + 1, :])
                    sc = jnp.where(causal, sc, NEG)
                    m_prev = m_sc[h]
                    m_new = jnp.maximum(m_prev, jnp.max(sc, axis=1, keepdims=True))
                    alpha = jnp.exp(m_prev - m_new)
                    pm = jnp.exp(sc - _wide(m_new))
                    l_sc[h] = alpha * l_sc[h] + jnp.sum(pm, axis=1, keepdims=True)
                    m_sc[h] = m_new
                    pv = jnp.dot(pm.astype(BF16), v2, preferred_element_type=F32)
                    acc2 = jnp.where(msk, alpha * acc2 + pv, acc2)
                acc_sc[:, lanes] = acc2

        @pl.when(j == i)
        def _():
            left = lax.broadcasted_iota(jnp.int32, (1, LANES), 1) < HEAD
            for p in range(N_PAIRS):
                lanes = slice(p * LANES, (p + 1) * LANES)
                inv = jnp.where(left, 1.0 / l_sc[2 * p], 1.0 / l_sc[2 * p + 1])
                o = acc_sc[:, lanes] * inv
                o_ref[:, lanes] = o
                gate = g_ref[:, lanes]
                mix_ref[:, lanes] = o * (gate * _sigmoid(gate))
            for h in range(N_HEADS):
                lse_ref[h] = m_sc[h] + jnp.log(l_sc[h])

    qblk = pl.BlockSpec((t, D_HALF), lambda i, j: (i, 0))
    kblk = pl.BlockSpec((t, D_HALF), lambda i, j: (jnp.minimum(i, j), 0))
    return pl.pallas_call(
        body, name="fox_attn_fwd", grid=(nblk, nblk),
        in_specs=[qblk, kblk, kblk, pl.BlockSpec((N_HEADS, t, LANES), lambda i, j: (0, i, 0)),
                  pl.BlockSpec((N_HEADS, t), lambda i, j: (0, jnp.minimum(i, j))),
                  pl.BlockSpec((t, D_HALF), lambda i, j: (i, 3))],
        out_specs=[qblk, qblk, pl.BlockSpec((N_HEADS, t, LANES), lambda i, j: (0, i, 0))],
        out_shape=[jax.ShapeDtypeStruct((s, D_HALF), F32), jax.ShapeDtypeStruct((s, D_HALF), F32),
                   jax.ShapeDtypeStruct((N_HEADS, s, LANES), F32)],
        scratch_shapes=[pltpu.VMEM((N_HEADS, t, LANES), F32), pltpu.VMEM((N_HEADS, t, LANES), F32),
                        pltpu.VMEM((t, D_HALF), F32)],
        compiler_params=_params("parallel", "arbitrary"),
    )(q, k, v, cc, cr, u_b)


def _fox_post_bwd(dmix, o, u_b, tm=256):
    s = o.shape[0]

    def body(dm_ref, o_ref, g_ref, do_ref, dg_ref):
        gate = g_ref[...]
        sg = _sigmoid(gate)
        dm = dm_ref[...]
        do_ref[...] = (dm * (gate * sg)).astype(BF16)
        dg_ref[...] = dm * o_ref[...] * (sg * (1.0 + gate * (1.0 - sg)))

    blk = pl.BlockSpec((tm, D_HALF), lambda i: (i, 0))
    return pl.pallas_call(
        body, name="fox_post_bwd", grid=(s // tm,),
        in_specs=[blk, blk, pl.BlockSpec((tm, D_HALF), lambda i: (i, 3))], out_specs=[blk] * 2,
        out_shape=[jax.ShapeDtypeStruct((s, D_HALF), BF16), jax.ShapeDtypeStruct((s, D_HALF), F32)],
        compiler_params=_params("parallel"),
    )(dmix, o, u_b)


def _wide(x):
    return jnp.concatenate([x, x], axis=1)


def _attn_probs(q2, k2, v2, do2, msk, causal, bias, lse_rows):
    zero = jnp.zeros_like(q2)
    qh = jnp.where(msk, q2, zero)
    doh = jnp.where(msk, do2, zero)
    sc = jnp.where(causal, _dot_nt(qh, k2) + bias, NEG)
    pm = jnp.exp(sc - _wide(lse_rows))
    dp = _dot_nt(doh, v2)
    return qh, doh, pm, dp


def _attn_bwd_rowdot(q, k, v, do, lse, cc, cr):
    s = q.shape[0]
    t = ATT_T
    nblk = s // t

    def body(q_ref, k_ref, v_ref, do_ref, lse_ref, cc_ref, cr_ref, dd_ref, acc):
        i = pl.program_id(0)
        j = pl.program_id(1)

        @pl.when(j == 0)
        def _():
            acc[...] = jnp.zeros_like(acc)

        @pl.when(j <= i)
        def _():
            row = i * t + lax.broadcasted_iota(jnp.int32, (t, t), 0)
            col = j * t + lax.broadcasted_iota(jnp.int32, (t, t), 1)
            causal = row >= col
            left = lax.broadcasted_iota(jnp.int32, (1, LANES), 1) < HEAD
            for p in range(N_PAIRS):
                lanes = slice(p * LANES, (p + 1) * LANES)
                q2, k2, v2, do2 = q_ref[:, lanes], k_ref[:, lanes], v_ref[:, lanes], do_ref[:, lanes]
                for e in range(2):
                    h = 2 * p + e
                    msk = left if e == 0 else jnp.logical_not(left)
                    bias = _wide(cc_ref[h]) - cr_ref[h:h + 1, :]
                    _, _, pm, dp = _attn_probs(q2, k2, v2, do2, msk, causal, bias, lse_ref[h])
                    acc[h] += jnp.sum(pm * dp, axis=1, keepdims=True)

        @pl.when(j == i)
        def _():
            dd_ref[...] = acc[...]

    qblk = pl.BlockSpec((t, D_HALF), lambda i, j: (i, 0))
    qcol = pl.BlockSpec((N_HEADS, t, LANES), lambda i, j: (0, i, 0))
    kblk = pl.BlockSpec((t, D_HALF), lambda i, j: (jnp.minimum(i, j), 0))
    return pl.pallas_call(
        body, name="fox_attn_rowdot", grid=(nblk, nblk),
        in_specs=[qblk, kblk, kblk, qblk, qcol, qcol, pl.BlockSpec((N_HEADS, t), lambda i, j: (0, jnp.minimum(i, j)))],
        out_specs=qcol, out_shape=jax.ShapeDtypeStruct((N_HEADS, s, LANES), F32),
        scratch_shapes=[pltpu.VMEM((N_HEADS, t, LANES), F32)],
        compiler_params=_params("parallel", "arbitrary"),
    )(q, k, v, do, lse, cc, cr)


def _attn_bwd(q, k, v, do, lse, dd, cc, cr):
    s = q.shape[0]
    t = ATT_T
    nblk = s // t

    def body(q_ref, k_ref, v_ref, do_ref, lse_ref, dd_ref, cc_ref, cr_ref,
             dq_ref, dk_ref, dv_ref, dcr_ref, dk_sc, dv_sc, dcr_sc):
        j = pl.program_id(0)
        i = pl.program_id(1)

        @pl.when(jnp.logical_and(j == 0, i == 0))
        def _():
            dq_ref[...] = jnp.zeros_like(dq_ref)

        @pl.when(i == 0)
        def _():
            dk_sc[...] = jnp.zeros_like(dk_sc)
            dv_sc[...] = jnp.zeros_like(dv_sc)
            dcr_sc[...] = jnp.zeros_like(dcr_sc)

        @pl.when(i >= j)
        def _():
            row = i * t + lax.broadcasted_iota(jnp.int32, (t, t), 0)
            col = j * t + lax.broadcasted_iota(jnp.int32, (t, t), 1)
            causal = row >= col
            left = lax.broadcasted_iota(jnp.int32, (1, LANES), 1) < HEAD
            qrows = pl.ds(pl.multiple_of(i * t, t), t)
            for p in range(N_PAIRS):
                lanes = slice(p * LANES, (p + 1) * LANES)
                q2, k2, v2, do2 = q_ref[:, lanes], k_ref[:, lanes], v_ref[:, lanes], do_ref[:, lanes]
                zero = jnp.zeros_like(q2)
                dq2 = jnp.zeros((t, LANES), F32)
                dk2 = jnp.zeros((t, LANES), F32)
                dv2 = jnp.zeros((t, LANES), F32)
                for e in range(2):
                    h = 2 * p + e
                    msk = left if e == 0 else jnp.logical_not(left)
                    bias = _wide(cc_ref[h]) - cr_ref[h:h + 1, :]
                    qh, doh, pm, dp = _attn_probs(q2, k2, v2, do2, msk, causal, bias, lse_ref[h])
                    dsc = pm * (dp - _wide(dd_ref[h]))
                    dsb = dsc.astype(BF16)
                    dv2 += _dot_tn(pm.astype(BF16), doh)
                    dk2 += _dot_tn(dsb, qh)
                    dq2 += jnp.dot(dsb, jnp.where(msk, k2, zero), preferred_element_type=F32)
                    dcr_sc[h:h + 1, :] += -_colsum(dsc)
                dq_ref[qrows, lanes] += dq2 * ATT_SCALE
                dk_sc[:, lanes] += dk2
                dv_sc[:, lanes] += dv2

        @pl.when(i == nblk - 1)
        def _():
            dk_ref[...] = dk_sc[...]
            dv_ref[...] = dv_sc[...]
            dcr_ref[...] = dcr_sc[...]

    qblk = pl.BlockSpec((t, D_HALF), lambda j, i: (jnp.maximum(i, j), 0))
    qcol = pl.BlockSpec((N_HEADS, t, LANES), lambda j, i: (0, jnp.maximum(i, j), 0))
    kblk = pl.BlockSpec((t, D_HALF), lambda j, i: (j, 0))
    return pl.pallas_call(
        body, name="fox_attn_bwd", grid=(nblk, nblk),
        in_specs=[qblk, kblk, kblk, qblk, qcol, qcol, qcol, pl.BlockSpec((N_HEADS, t), lambda j, i: (0, j))],
        out_specs=[pl.BlockSpec((s, D_HALF), lambda j, i: (0, 0)), kblk, kblk,
                   pl.BlockSpec((N_HEADS, t), lambda j, i: (0, j))],
        out_shape=[jax.ShapeDtypeStruct((s, D_HALF), F32)] * 3 + [jax.ShapeDtypeStruct((N_HEADS, s), F32)],
        scratch_shapes=[pltpu.VMEM((t, D_HALF), F32), pltpu.VMEM((t, D_HALF), F32), pltpu.VMEM((N_HEADS, t), F32)],
        compiler_params=_params("arbitrary", "arbitrary"),
    )(q, k, v, do, lse, dd, cc, cr)


def _fox_prep_bwd(u_b, dq, dk, dv, dgate, dcum, fb, qg, kg, tm=256):
    s = u_b.shape[0]
    nb = s // tm

    def body(ub_ref, dq_ref, dk_ref, dv_ref, dg_ref, dc_ref, fb_ref, qg_ref, kg_ref,
             du_ref, dqg_ref, dkg_ref, dfb_ref, carry):
        i = pl.program_id(0)

        @pl.when(i == 0)
        def _():
            carry[...] = jnp.zeros_like(carry)
            dqg_ref[...] = jnp.zeros_like(dqg_ref)
            dkg_ref[...] = jnp.zeros_like(dkg_ref)
            dfb_ref[...] = jnp.zeros_like(dfb_ref)

        bd = _head_ones()
        for lo, g_ref, d_ref, dgain_ref in ((0, qg_ref, dq_ref, dqg_ref), (512, kg_ref, dk_ref, dkg_ref)):
            gain = g_ref[...]
            xh, rinv, _ = _head_rms(ub_ref[:, lo:lo + 512], gain, bd)
            dn = d_ref[...]
            dgain_ref[...] += _colsum(dn * xh)
            dxh = dn * gain
            du_ref[:, lo:lo + 512] = rinv * (dxh - xh * (_head_sum(dxh * xh, bd) * (1.0 / HEAD)))
        du_ref[:, 1024:1536] = dv_ref[...]
        du_ref[:, 1536:2048] = dg_ref[...]
        lane = lax.broadcasted_iota(jnp.int32, (1, LANES), 1)
        dc = dc_ref[...]
        dlogf = _exact_dot(dc, _tri(tm, False), ones_first=True) + carry[...]
        carry[...] += _colsum(dc)
        fl = ub_ref[:, 2048:2176] + fb_ref[...]
        dfl = jnp.where(lane < N_HEADS, dlogf * (1.0 - _sigmoid(fl)), 0.0)
        du_ref[:, 2048:2176] = dfl
        dfb_ref[...] += _colsum(dfl)

    rev = lambda w: pl.BlockSpec((tm, w), lambda i: (nb - 1 - i, 0))
    vec = lambda w: pl.BlockSpec((1, w), lambda i: (0, 0))
    return pl.pallas_call(
        body, name="fox_prep_bwd", grid=(nb,),
        in_specs=[rev(SEC)] + [rev(D_HALF)] * 4 + [rev(LANES), vec(LANES), vec(D_HALF), vec(D_HALF)],
        out_specs=[rev(SEC), vec(D_HALF), vec(D_HALF), vec(LANES)],
        out_shape=[jax.ShapeDtypeStruct((s, SEC), F32), jax.ShapeDtypeStruct((1, D_HALF), F32),
                   jax.ShapeDtypeStruct((1, D_HALF), F32), jax.ShapeDtypeStruct((1, LANES), F32)],
        scratch_shapes=[pltpu.VMEM((1, LANES), F32)],
        compiler_params=_params("arbitrary"),
    )(u_b, dq, dk, dv, dgate, dcum, fb, qg, kg)


def _merge(mix_a, mix_b, u_g, x, tgt, wa, wb, wo, fg, tm=256):
    s, d = x.shape

    def body(ma_ref, mb_ref, ug_ref, x_ref, t_ref, wa_ref, wb_ref, wo_ref, fg_ref,
             dx2_ref, dma_ref, dmb_ref, dug_ref, dwa_ref, dwb_ref, dwo_ref, dfg_ref, loss_ref):
        i = pl.program_id(0)

        @pl.when(i == 0)
        def _():
            for ref in (dwa_ref, dwb_ref, dwo_ref, dfg_ref, loss_ref):
                ref[...] = jnp.zeros_like(ref)

        wa_v, wb_v, wo_v, fg_v = wa_ref[...], wb_ref[...], wo_ref[...], fg_ref[...]
        ma = ma_ref[...].astype(BF16)
        mb = mb_ref[...].astype(BF16)
        ya = jnp.dot(ma, wa_v, preferred_element_type=F32)
        yb = jnp.dot(mb, wb_v, preferred_element_type=F32)
        sa = _sigmoid(ug_ref[:, 0:d])
        sb = _sigmoid(ug_ref[:, d:2 * d])
        merged = (sa * ya + sb * yb).astype(BF16)
        x2 = x_ref[...] + jnp.dot(merged, wo_v, preferred_element_type=F32)
        r2 = lax.rsqrt(jnp.mean(x2 * x2, axis=-1, keepdims=True) + RMS_EPS)
        x2h = x2 * r2
        err = x2h * fg_v - t_ref[...]
        loss_ref[...] += _colsum(err * err)
        dy = err * (1.0 / d)
        dfg_ref[...] += _colsum(dy * x2h)
        dx2h = dy * fg_v
        dx2 = r2 * (dx2h - x2h * jnp.mean(dx2h * x2h, axis=-1, keepdims=True))
        dx2_ref[...] = dx2
        dx2b = dx2.astype(BF16)
        dmerged = _dot_nt(dx2b, wo_v)
        dwo_ref[...] += _dot_tn(merged, dx2b)
        dya = dmerged * sa
        dyb = dmerged * sb
        dug_ref[:, 0:d] = dya * ya * (1.0 - sa)
        dug_ref[:, d:2 * d] = dyb * yb * (1.0 - sb)
        dyab = dya.astype(BF16)
        dybb = dyb.astype(BF16)
        dma_ref[...] = _dot_nt(dyab, wa_v)
        dmb_ref[...] = _dot_nt(dybb, wb_v)
        dwa_ref[...] += _dot_tn(ma, dyab)
        dwb_ref[...] += _dot_tn(mb, dybb)

    row = lambda w: pl.BlockSpec((tm, w), lambda i: (i, 0))
    full = lambda a: pl.BlockSpec(a.shape, lambda i: (0, 0))
    fshape = lambda a: jax.ShapeDtypeStruct(a.shape, F32)
    return pl.pallas_call(
        body, name="merge_fwd_bwd", grid=(s // tm,),
        in_specs=[row(D_HALF), row(D_HALF), row(GATE_COLS), row(d), row(d), full(wa), full(wb), full(wo), full(fg)],
        out_specs=[row(d), row(D_HALF), row(D_HALF), row(GATE_COLS), full(wa), full(wb), full(wo), full(fg), full(fg)],
        out_shape=[jax.ShapeDtypeStruct((s, d), F32), jax.ShapeDtypeStruct((s, D_HALF), F32),
                   jax.ShapeDtypeStruct((s, D_HALF), F32), jax.ShapeDtypeStruct((s, GATE_COLS), F32),
                   fshape(wa), fshape(wb), fshape(wo), fshape(fg), fshape(fg)],
        compiler_params=_params("arbitrary"),
    )(mix_a, mix_b, u_g, x, tgt, wa, wb, wo, fg)


def _lora_weight(w_up, a_up):
    z = jnp.zeros((LORA, D_HALF), w_up.dtype)
    return jnp.concatenate([jnp.concatenate([w_up, z], axis=1), jnp.concatenate([z, a_up], axis=1)], axis=0)


def _device_grads(x, tgt, p, w_a, w_up, a_up, late_weights, fwd_exchange=None, bwd_exchange=None):
    wl = _lora_weight(w_up, a_up)
    rk = p["r_k"].reshape(1, D_HALF)
    fb = jnp.pad(p["f_bias"], ((0, 0), (0, LANES - N_HEADS)))
    qg = jnp.tile(p["q_norm_g"], (1, N_HEADS))
    kg = jnp.tile(p["k_norm_g"], (1, N_HEADS))
    fg = p["final_norm_g"].reshape(1, D_MODEL)
    mixer = (p["shift_mu"], wl, p["w0"], p["a0"], p["k_k"], p["k_a"])

    h = _rmsnorm_in(x, p["norm_g"])
    u_a = _matmul_nn(h, w_a, "inproj_rwkv")
    r, dec, k, v, av, bv, gate_a = _rwkv_prep(u_a, *mixer)
    y, st, arrived = _wkv_fwd(r, dec, k, av, bv, v, fwd_exchange)
    mix_a = _rwkv_post(y, r, k, v, gate_a, p["lnx_w"], p["lnx_b"], rk)

    w_b, w_g, w_out_a, w_out_b, w_out = late_weights(arrived)
    u_b = _matmul_nn(h, w_b, "inproj_fox")
    u_g = _matmul_nn(h, w_g, "inproj_gate")
    q, kn, vb, cc, cr = _fox_prep(u_b, fb, qg, kg)
    o, mix_b, lse = _attn_fwd(q, kn, vb, cc, cr, u_b)

    dx2, dmix_a, dmix_b, du_g, dwa, dwb, dwo, dfg, loss_vec = _merge(
        mix_a, mix_b, u_g, x, tgt, w_out_a, w_out_b, w_out, fg)

    do, dgate_b = _fox_post_bwd(dmix_b, o, u_b)
    dd = _attn_bwd_rowdot(q, kn, vb, do, lse, cc, cr)
    dq, dk_att, dv_att, dcr = _attn_bwd(q, kn, vb, do, lse, dd, cc, cr)
    dcum = jnp.pad(dcr.T, ((0, 0), (0, LANES - N_HEADS)))
    du_b, dqg, dkg, dfb = _fox_prep_bwd(u_b, dq, dk_att, dv_att, dgate_b, dcum, fb, qg, kg)
    h_t = h.T
    dw_b = _matmul_tn_acc(h_t, du_b, "dw_fox")
    dw_g = _matmul_tn_acc(h_t, du_g, "dw_gate")

    dy, dr_b, dk_b, dv_b, dgate_a, dlw, dlb, drk = _rwkv_post_bwd(
        dmix_a, y, r, k, v, gate_a, p["lnx_w"], p["lnx_b"], rk)
    scan_grads, sent = _wkv_bwd(r, dec, k, av, bv, v, dy, st,
                                bwd_exchange(dw_b, dw_g, dwa, dwb, dwo) if bwd_exchange else None)
    du_a, dmu, dwl, dw0, da0, dkkw, dkaw = _rwkv_prep_bwd(u_a, (*scan_grads, dr_b, dk_b, dv_b, dgate_a), *mixer)
    dw_a = _matmul_tn_acc(h_t, du_a, "dw_rwkv")
    grad_x, dnorm_g = _inproj_bwd(du_a, du_b, du_g, w_a, w_b, w_g, x, dx2, p["norm_g"])

    grads = dict(
        norm_g=dnorm_g, w_in=(dw_a, dw_b, dw_g), shift_mu=dmu,
        w_lora_up=dwl[:LORA, :D_HALF], w0=dw0, a_lora_up=dwl[LORA:, D_HALF:], a0=da0, k_k=dkkw, k_a=dkaw,
        r_k=drk.reshape(1, N_HEADS, HEAD), lnx_w=dlw, lnx_b=dlb, f_bias=dfb[:, :N_HEADS],
        q_norm_g=dqg.reshape(N_HEADS, HEAD).sum(axis=0, keepdims=True),
        k_norm_g=dkg.reshape(N_HEADS, HEAD).sum(axis=0, keepdims=True),
        w_out_a=dwa, w_out_b=dwb, w_out=dwo, final_norm_g=dfg.reshape(D_MODEL))
    return loss_vec, grad_x, grads, sent


CHIP_FLIPS = ((1, 0), (0, 1), (1, 1))
ANY = pl.BlockSpec(memory_space=pl.ANY)


def _position():
    return lax.axis_index("x"), lax.axis_index("y"), lax.axis_index("c")


def _flip(v, f):
    return 1 - v if f else v


def _both(a, b):
    if a is None:
        return b
    return a if b is None else jnp.logical_and(a, b)


def _when(cond, fn):
    if cond is None:
        fn()
    else:
        pl.when(cond)(fn)


class _Moves:
    def __init__(self, send_sems, recv_sems, local_sems):
        self.send_sems, self.recv_sems, self.local_sems = send_sems, recv_sems, local_sems
        self.remote, self.local = [], []

    def send(self, src, dst, peer, landing, send_if=None, recv_if=None):
        k = len(self.remote)
        sems = dict(send_sem=self.send_sems.at[k], recv_sem=self.recv_sems.at[k], device_id=peer, device_id_type=MESH)
        out = pltpu.make_async_remote_copy(src_ref=src, dst_ref=dst, **sems)
        arrival = pltpu.make_async_remote_copy(src_ref=src, dst_ref=landing, **sems)
        self.remote.append((out, arrival, send_if, recv_if))

    def copy(self, src, dst, cond=None):
        cp = pltpu.make_async_copy(src, dst, self.local_sems.at[len(self.local)])
        self.local.append((cp, cond))

    def start(self, also=None):
        for cp, cond in self.local:
            _when(_both(also, cond), cp.start)
        for out, _, send_if, _ in self.remote:
            _when(_both(also, send_if), out.start)

    def wait_arrivals(self, also=None):
        for _, arrival, _, recv_if in self.remote:
            _when(_both(also, recv_if), arrival.wait_recv)

    def wait_sent(self, also=None):
        for out, _, send_if, _ in self.remote:
            _when(_both(also, send_if), out.wait_send)
        for cp, cond in self.local:
            _when(_both(also, cond), cp.wait)

    def wait(self, also=None):
        self.wait_arrivals(also)
        self.wait_sent(also)


class _Exchange:
    def __init__(self, operands, out_shapes, n_remote, n_local, build, n_relay=0, relay=None):
        self.operands, self.out_shapes = list(operands), list(out_shapes)
        self.n_remote, self.n_local, self.build = n_remote, n_local, build
        self.n_relay, self.relay = n_relay, relay

    def scratch(self):
        return [pltpu.SemaphoreType.DMA((self.n_remote,)), pltpu.SemaphoreType.DMA((self.n_remote,)),
                pltpu.SemaphoreType.DMA((max(self.n_local, 1),))]

    def moves(self, in_refs, out_refs, sems):
        mv = _Moves(*sems)
        self.build(mv, in_refs, out_refs)
        return mv

    def run_alone(self, name):
        n_in, n_out = len(self.operands), len(self.out_shapes)
        relay_scratch = [pltpu.SemaphoreType.DMA((self.n_relay,))] * 2 if self.relay else []

        def body(*refs):
            ins, outs, sems = refs[:n_in], refs[n_in:n_in + n_out], refs[n_in + n_out:]
            mv = self.moves(ins, outs, sems[:3])
            mv.start()
            mv.wait_arrivals()
            if self.relay:
                passed = _Moves(sems[3], sems[4], None)
                self.relay(passed, ins, outs)
                passed.start()
                passed.wait()
            mv.wait_sent()

        return pl.pallas_call(
            body, name=name, in_specs=[ANY] * n_in, out_specs=[ANY] * n_out, out_shape=self.out_shapes,
            scratch_shapes=self.scratch() + relay_scratch, compiler_params=pltpu.CompilerParams(has_side_effects=True),
        )(*self.operands)


def _gather_exchange(half, w_in_shard, others, split=False):
    n = len(others)
    rows = w_in_shard.shape[0] // 2

    def build(mv, ins, outs):
        x, y, c = _position()
        me = 2 * x + y
        part = pl.ds(c * rows, rows) if split else slice(None)
        mv.copy(ins[0], outs[0].at[y], cond=(x == half))
        for t in range(n):
            mv.copy(ins[1 + t], outs[1 + t].at[me])
        for fx, fy in CHIP_FLIPS:
            px, py = _flip(x, fx), _flip(y, fy)
            peer = (px, py, c)
            mv.send(ins[0].at[part], outs[0].at[y, part], peer, landing=outs[0].at[py, part],
                    send_if=(x == half), recv_if=(px == half))
            for t in range(n):
                mv.send(ins[1 + t], outs[1 + t].at[me], peer, landing=outs[1 + t].at[2 * px + py])

    def relay(mv, ins, outs):
        x, y, c = _position()
        mine, theirs = pl.ds(c * rows, rows), pl.ds((1 - c) * rows, rows)
        for fx, fy in CHIP_FLIPS:
            px, py = _flip(x, fx), _flip(y, fy)
            came = (px == half)
            mv.send(outs[0].at[py, mine], outs[0].at[py, mine], (x, y, 1 - c), landing=outs[0].at[py, theirs],
                    send_if=came, recv_if=came)

    shapes = [jax.ShapeDtypeStruct((2,) + w_in_shard.shape, w_in_shard.dtype)]
    shapes += [jax.ShapeDtypeStruct((N_CHIPS,) + a.shape, a.dtype) for a in others]
    return _Exchange([w_in_shard] + list(others), shapes, len(CHIP_FLIPS) * (1 + n), 1 + n, build,
                     n_relay=len(CHIP_FLIPS), relay=relay if split else None)


def _scatter_exchange(half, w_in_blocks, stacks):
    n = len(stacks)

    def build(mv, ins, outs):
        x, y, c = _position()
        for f, (fx, fy) in enumerate(CHIP_FLIPS):
            px, py = _flip(x, fx), _flip(y, fy)
            peer = (px, py, c)
            mv.send(ins[0].at[py], outs[0].at[f], peer, landing=outs[0].at[f], send_if=(px == half), recv_if=(x == half))
            for t in range(n):
                mv.send(ins[1 + t].at[2 * px + py], outs[1 + t].at[f], peer, landing=outs[1 + t].at[f])

    shapes = [jax.ShapeDtypeStruct((len(CHIP_FLIPS),) + a.shape[1:], a.dtype) for a in [w_in_blocks] + list(stacks)]
    return _Exchange([w_in_blocks] + list(stacks), shapes, len(CHIP_FLIPS) * (1 + n), 0, build)


def _swap_sibling(tensors):
    n = len(tensors)

    def body(*refs):
        ins, outs = refs[:n], refs[n:2 * n]
        send_sems, recv_sems = refs[2 * n:]
        x, y, c = _position()
        copies = [pltpu.make_async_remote_copy(
            src_ref=ins[t], dst_ref=outs[t], send_sem=send_sems.at[t], recv_sem=recv_sems.at[t],
            device_id=(x, y, 1 - c), device_id_type=MESH) for t in range(n)]
        for cp in copies:
            cp.start()
        for cp in copies:
            cp.wait_recv()
        for cp in copies:
            cp.wait_send()

    return pl.pallas_call(
        body, name="swap_sibling", in_specs=[ANY] * n, out_specs=[ANY] * n,
        out_shape=[jax.ShapeDtypeStruct(a.shape, a.dtype) for a in tensors],
        scratch_shapes=[pltpu.SemaphoreType.DMA((n,)), pltpu.SemaphoreType.DMA((n,))],
        compiler_params=pltpu.CompilerParams(has_side_effects=True),
    )(*tensors)


def _allreduce_small(slab):
    stages = 3

    def body(x_ref, o_ref, buf, send_sems, recv_sems):
        x, y, c = _position()
        peers = ((1 - x, y, c), (x, 1 - y, c), (x, y, 1 - c))
        o_ref[...] = x_ref[...]
        for k, peer in enumerate(peers):
            cp = pltpu.make_async_remote_copy(src_ref=o_ref, dst_ref=buf.at[k], send_sem=send_sems.at[k],
                                              recv_sem=recv_sems.at[k], device_id=peer, device_id_type=MESH)
            cp.start()
            cp.wait()
            o_ref[...] = o_ref[...] + buf[k]

    return pl.pallas_call(
        body, name="allreduce_small",
        in_specs=[pl.BlockSpec(memory_space=pltpu.VMEM)], out_specs=pl.BlockSpec(memory_space=pltpu.VMEM),
        out_shape=jax.ShapeDtypeStruct(slab.shape, slab.dtype),
        scratch_shapes=[pltpu.VMEM((stages,) + slab.shape, slab.dtype),
                        pltpu.SemaphoreType.DMA((stages,)), pltpu.SemaphoreType.DMA((stages,))],
        compiler_params=pltpu.CompilerParams(has_side_effects=True),
    )(slab)


def _row_tile(r):
    return min(r, 256)


def _sum4(stack, recv, me):
    _, r, c = stack.shape
    tr = _row_tile(r)

    def body(me_ref, own_ref, recv_ref, o_ref):
        o_ref[...] = (((own_ref[...] + recv_ref[0].astype(F32)) + recv_ref[1].astype(F32))
                      + recv_ref[2].astype(F32))

    return pl.pallas_call(
        body, name="sum_partials",
        grid_spec=pltpu.PrefetchScalarGridSpec(
            num_scalar_prefetch=1, grid=(r // tr,),
            in_specs=[pl.BlockSpec((None, tr, c), lambda i, me_ref: (me_ref[0], i, 0)),
                      pl.BlockSpec((len(CHIP_FLIPS), tr, c), lambda i, me_ref: (0, i, 0))],
            out_specs=pl.BlockSpec((tr, c), lambda i, me_ref: (i, 0))),
        out_shape=jax.ShapeDtypeStruct((r, c), F32), compiler_params=_params("parallel"),
    )(me, stack, recv)


def _adamw_math(w, g, m, v):
    m = ADAM_B1 * m + (1.0 - ADAM_B1) * g
    v = ADAM_B2 * v + (1.0 - ADAM_B2) * (g * g)
    m_hat = m / (1.0 - ADAM_B1 ** ADAM_STEP)
    v_hat = v / (1.0 - ADAM_B2 ** ADAM_STEP)
    delta = -ADAM_LR * (m_hat / (jnp.sqrt(v_hat) + ADAM_EPS) + ADAM_WD * w)
    return delta, m, v


def _adamw(w, m, v, g_parts, name):
    r, c = w.shape
    tr = _row_tile(r)
    n = len(g_parts)

    def body(*refs):
        w_ref, m_ref, v_ref = refs[:3]
        g_refs = refs[3:3 + n]
        g_out, d_out, m_out, v_out = refs[3 + n:]
        g = g_refs[0][...]
        for ref in g_refs[1:]:
            g = g + ref[...]
        g_out[...] = g
        d_out[...], m_out[...], v_out[...] = _adamw_math(w_ref[...], g, m_ref[...], v_ref[...])

    blk = pl.BlockSpec((tr, c), lambda i: (i, 0))
    return pl.pallas_call(
        body, name=name, grid=(r // tr,), in_specs=[blk] * (3 + n), out_specs=[blk] * 4,
        out_shape=[jax.ShapeDtypeStruct((r, c), F32)] * 4, compiler_params=_params("parallel"),
    )(w, m, v, *g_parts)


SHARDED = ("w_in", "w_lora_up", "a_lora_up", "w_out_a", "w_out_b", "w_out")
ROW_SHARDED = ("w_out",)
SMALL = ("norm_g", "shift_mu", "w0", "a0", "k_k", "k_a", "r_k", "lnx_w", "lnx_b", "f_bias", "q_norm_g", "k_norm_g",
         "final_norm_g")
WEIGHTS = ("norm_g", "w_in", "shift_mu", "w_lora_up", "w0", "a_lora_up", "a0", "k_k", "k_a", "r_k", "lnx_w", "lnx_b",
           "f_bias", "q_norm_g", "k_norm_g", "w_out_a", "w_out_b", "w_out", "final_norm_g")
SLAB_ROWS = 16
SLAB_COLS = SEC


def _to_slab(named, extra=None):
    rows = [jnp.pad(named[n].reshape(1, -1), ((0, 0), (0, SLAB_COLS - named[n].size))) for n in SMALL]
    if extra is not None:
        rows.append(jnp.pad(extra.reshape(1, -1), ((0, 0), (0, SLAB_COLS - extra.size))))
    rows.append(jnp.zeros((SLAB_ROWS - len(rows), SLAB_COLS), F32))
    return jnp.concatenate(rows, axis=0)


def _from_slab(slab, shapes):
    return {n: slab[i, :math.prod(shapes[n])].reshape(shapes[n]) for i, n in enumerate(SMALL)}


def _by_chip(g, name):
    if name in ROW_SHARDED:
        return g.reshape(N_CHIPS, g.shape[0] // N_CHIPS, g.shape[1])
    r, c = g.shape
    return g.reshape(r, N_CHIPS, c // N_CHIPS).transpose(1, 0, 2)


def _from_chips(stack, name):
    if name in ROW_SHARDED:
        return stack.reshape(-1, stack.shape[2])
    _, r, c = stack.shape
    return stack.transpose(1, 0, 2).reshape(r, N_CHIPS * c)


def kernel(x, norm_g, w_in, shift_mu, w_lora_up, w0, a_lora_up, a0, k_k, k_a, r_k, lnx_w, lnx_b, f_bias, q_norm_g, k_norm_g, w_out_a, w_out_b, w_out, final_norm_g, loss_target, m_norm_g, m_w_in, m_shift_mu, m_w_lora_up, m_w0, m_a_lora_up, m_a0, m_k_k, m_k_a, m_r_k, m_lnx_w, m_lnx_b, m_f_bias, m_q_norm_g, m_k_norm_g, m_w_out_a, m_w_out_b, m_w_out, m_final_norm_g, v_norm_g, v_w_in, v_shift_mu, v_w_lora_up, v_w0, v_a_lora_up, v_a0, v_k_k, v_k_a, v_r_k, v_lnx_w, v_lnx_b, v_f_bias, v_q_norm_g, v_k_norm_g, v_w_out_a, v_w_out_b, v_w_out, v_final_norm_g):
    w = dict(norm_g=norm_g, w_in=w_in, shift_mu=shift_mu, w_lora_up=w_lora_up, w0=w0, a_lora_up=a_lora_up, a0=a0,
             k_k=k_k, k_a=k_a, r_k=r_k, lnx_w=lnx_w, lnx_b=lnx_b, f_bias=f_bias, q_norm_g=q_norm_g,
             k_norm_g=k_norm_g, w_out_a=w_out_a, w_out_b=w_out_b, w_out=w_out, final_norm_g=final_norm_g)
    m = dict(norm_g=m_norm_g, w_in=m_w_in, shift_mu=m_shift_mu, w_lora_up=m_w_lora_up, w0=m_w0,
             a_lora_up=m_a_lora_up, a0=m_a0, k_k=m_k_k, k_a=m_k_a, r_k=m_r_k, lnx_w=m_lnx_w, lnx_b=m_lnx_b,
             f_bias=m_f_bias, q_norm_g=m_q_norm_g, k_norm_g=m_k_norm_g, w_out_a=m_w_out_a, w_out_b=m_w_out_b,
             w_out=m_w_out, final_norm_g=m_final_norm_g)
    v = dict(norm_g=v_norm_g, w_in=v_w_in, shift_mu=v_shift_mu, w_lora_up=v_w_lora_up, w0=v_w0,
             a_lora_up=v_a_lora_up, a0=v_a0, k_k=v_k_k, k_a=v_k_a, r_k=v_r_k, lnx_w=v_lnx_w, lnx_b=v_lnx_b,
             f_bias=v_f_bias, q_norm_g=v_q_norm_g, k_norm_g=v_k_norm_g, w_out_a=v_w_out_a, w_out_b=v_w_out_b,
             w_out=v_w_out, final_norm_g=v_final_norm_g)
    shapes = {n: w[n].shape for n in WEIGHTS}

    shard = {n: w[n][0].astype(BF16) for n in SHARDED}
    late = ("w_out_a", "w_out_b", "w_out")
    w01, up_stack, aup_stack = _gather_exchange(
        0, shard["w_in"], [shard["w_lora_up"], shard["a_lora_up"]], split=True).run_alone("gather_early")
    w_a = jnp.concatenate([w01[0], w01[1][:, :A_TAIL]], axis=1)

    def late_weights(arrived):
        w23 = arrived[0]
        w_b = jnp.concatenate([w01[1][:, A_TAIL:], w23[0][:, :B_TAIL], jnp.zeros((D_MODEL, SEC - FOX_REAL), BF16)], axis=1)
        w_g = jnp.concatenate([w23[0][:, B_TAIL:], w23[1]], axis=1)
        return (w_b, w_g, *[_from_chips(s, n) for n, s in zip(late, arrived[1:])])

    kept = {}

    def bwd_exchange(dw_b, dw_g, dwa, dwb, dwo):
        kept["w_in_1"] = jnp.stack([jnp.concatenate([dw_b[:, B_HEAD:FOX_REAL], dw_g[:, :G_HEAD]], axis=1), dw_g[:, G_HEAD:]])
        kept.update({n: _by_chip(g, n) for n, g in zip(late, (dwa, dwb, dwo))})
        return _scatter_exchange(1, kept["w_in_1"].astype(BF16), [kept[n].astype(BF16) for n in late])

    small = {n: w[n] for n in SMALL}
    loss_vec, grad_x, grads, sent = _device_grads(
        x[0], loss_target[0], small, w_a, _from_chips(up_stack, "w_lora_up"), _from_chips(aup_stack, "a_lora_up"),
        late_weights, _gather_exchange(1, shard["w_in"], [shard[n] for n in late]), bwd_exchange)

    total = _allreduce_small(_to_slab(grads, extra=loss_vec))
    loss = (0.5 / D_MODEL) * jnp.sum(total[len(SMALL)])
    slab_g, slab_d, slab_m, slab_v = _adamw(_to_slab(w), _to_slab(m), _to_slab(v), [total], "adamw_small")
    out_g, out_d, out_m, out_v = (_from_slab(s, shapes) for s in (total, slab_d, slab_m, slab_v))
    del slab_g

    dw_a, dw_b, _ = grads["w_in"]
    w_in_0 = jnp.stack([dw_a[:, :SHARD_COLS], jnp.concatenate([dw_a[:, SHARD_COLS:], dw_b[:, :B_HEAD]], axis=1)])
    loras = ("w_lora_up", "a_lora_up")
    kept.update({n: _by_chip(grads[n], n) for n in loras})
    sent_late = _scatter_exchange(0, w_in_0.astype(BF16), [kept[n].astype(BF16) for n in loras]).run_alone("scatter_late")
    xpos, ypos, _ = _position()
    y_index = ypos.astype(jnp.int32).reshape(1)
    me = (2 * xpos + ypos).astype(jnp.int32).reshape(1)
    core_sum = {"w_in": lax.cond(xpos == 1, lambda: _sum4(kept["w_in_1"], sent[0], y_index),
                                 lambda: _sum4(w_in_0, sent_late[0], y_index))}
    core_sum.update({n: _sum4(kept[n], r, me) for n, r in zip(late, sent[1:])})
    core_sum.update({n: _sum4(kept[n], r, me) for n, r in zip(loras, sent_late[1:])})
    sibling_sums = _swap_sibling([core_sum[n] for n in SHARDED])
    for n, theirs in zip(SHARDED, sibling_sums):
        g, d, m2, v2 = _adamw(w[n][0], m[n][0], v[n][0], [core_sum[n], theirs], "adamw_" + n)
        out_g[n], out_d[n], out_m[n], out_v[n] = (a.reshape(shapes[n]) for a in (g, d, m2, v2))

    return (loss, grad_x.reshape(x.shape), *[out_g[n] for n in WEIGHTS], *[out_d[n] for n in WEIGHTS],
            *[out_m[n] for n in WEIGHTS], *[out_v[n] for n in WEIGHTS])
```

```python
import functools
import math

import jax
import jax.numpy as jnp
from jax import lax
from jax.experimental import pallas as pl
from jax.experimental.pallas import tpu as pltpu

F32 = jnp.float32
BF16 = jnp.bfloat16

D_MODEL = 1024
D_HALF = 512
HEAD = 64
N_HEADS = 8
LORA = 64
RWKV_COLS = 2176
FOX_REAL = 2056
SEC = 2176
GATE_COLS = 2048
IN_COLS = 6280
N_CHIPS = 4
SHARD_COLS = IN_COLS // N_CHIPS
A_TAIL = RWKV_COLS - SHARD_COLS
B_HEAD = SHARD_COLS - A_TAIL
B_TAIL = FOX_REAL - B_HEAD
G_HEAD = SHARD_COLS - B_TAIL
RMS_EPS = 1e-6
LNX_EPS = 64e-5
ATT_SCALE = HEAD ** -0.5
NEG = -1e30

ADAM_LR = 0.001
ADAM_B1 = 0.9
ADAM_B2 = 0.999
ADAM_EPS = 1e-08
ADAM_WD = 0.01
ADAM_STEP = 10

LANES = 128
SUBLANES = 8
VMEM_LIMIT = 56 * 1024 * 1024
MESH = pl.DeviceIdType.MESH


def _params(*sem):
    return pltpu.CompilerParams(dimension_semantics=sem if sem else None, vmem_limit_bytes=VMEM_LIMIT)


def _sigmoid(x):
    return 1.0 / (1.0 + jnp.exp(-x))


def _log_sigmoid(x):
    return jnp.minimum(x, 0.0) - jnp.log(1.0 + jnp.exp(-jnp.abs(x)))


def _head_ones():
    r = lax.broadcasted_iota(jnp.int32, (LANES, LANES), 0) >> 6
    c = lax.broadcasted_iota(jnp.int32, (LANES, LANES), 1) >> 6
    return (r == c).astype(BF16)


def _split3(x):
    hi = x.astype(BF16)
    r1 = x - hi.astype(F32)
    mid = r1.astype(BF16)
    lo = (r1 - mid.astype(F32)).astype(BF16)
    return hi, mid, lo


def _exact_dot(x, ones_bf16, ones_first=False):
    out = None
    for piece in _split3(x):
        if ones_first:
            t = jnp.dot(ones_bf16, piece, preferred_element_type=F32)
        else:
            t = jnp.dot(piece, ones_bf16, preferred_element_type=F32)
        out = t if out is None else out + t
    return out


def _head_sum(x, bd):
    n = x.shape[1] // LANES
    parts = [_exact_dot(x[:, i * LANES:(i + 1) * LANES], bd) for i in range(n)]
    return parts[0] if n == 1 else jnp.concatenate(parts, axis=1)


def _dot_nt(a, b):
    return lax.dot_general(a, b, (((1,), (1,)), ((), ())), preferred_element_type=F32)


def _dot_tn(a, b):
    return lax.dot_general(a, b, (((0,), (0,)), ((), ())), preferred_element_type=F32)


def _colsum(x):
    return jnp.sum(x, axis=0, keepdims=True)


def _rmsnorm_in(x, g, tm=512):
    s, d = x.shape

    def body(x_ref, g_ref, h_ref):
        xv = x_ref[...]
        r = lax.rsqrt(jnp.mean(xv * xv, axis=-1, keepdims=True) + RMS_EPS)
        h_ref[...] = (xv * r * g_ref[...]).astype(BF16)

    return pl.pallas_call(
        body, name="rmsnorm_in", grid=(s // tm,),
        in_specs=[pl.BlockSpec((tm, d), lambda i: (i, 0)), pl.BlockSpec((1, d), lambda i: (0, 0))],
        out_specs=pl.BlockSpec((tm, d), lambda i: (i, 0)),
        out_shape=jax.ShapeDtypeStruct((s, d), BF16), compiler_params=_params("parallel"),
    )(x, g)


def _matmul_nn(a, b, name, tm=512):
    m, k = a.shape
    n = b.shape[1]

    def body(a_ref, b_ref, o_ref):
        o_ref[...] = jnp.dot(a_ref[...], b_ref[...], preferred_element_type=F32)

    return pl.pallas_call(
        body, name=name, grid=(m // tm,),
        in_specs=[pl.BlockSpec((tm, k), lambda i: (i, 0)), pl.BlockSpec((k, n), lambda i: (0, 0))],
        out_specs=pl.BlockSpec((tm, n), lambda i: (i, 0)),
        out_shape=jax.ShapeDtypeStruct((m, n), F32), compiler_params=_params("parallel"),
    )(a, b)


def _matmul_tn_acc(at, b, name, tk=512):
    m, k = at.shape
    n = b.shape[1]

    def body(a_ref, b_ref, o_ref):
        j = pl.program_id(0)

        @pl.when(j == 0)
        def _():
            o_ref[...] = jnp.zeros_like(o_ref)

        o_ref[...] += jnp.dot(a_ref[...], b_ref[...].astype(BF16), preferred_element_type=F32)

    return pl.pallas_call(
        body, name=name, grid=(k // tk,),
        in_specs=[pl.BlockSpec((m, tk), lambda j: (0, j)), pl.BlockSpec((tk, n), lambda j: (j, 0))],
        out_specs=pl.BlockSpec((m, n), lambda j: (0, 0)),
        out_shape=jax.ShapeDtypeStruct((m, n), F32), compiler_params=_params("arbitrary"),
    )(at, b)


def _inproj_bwd(du_a, du_b, du_g, w_a, w_b, w_g, x, dx2, g, exchange=None, tm=256):
    s, d = x.shape
    nb = s // tm

    def body(*refs):
        ((da_ref, db_ref, dg_ref, wa_ref, wb_ref, wg_ref, x_ref, dx2_ref, g_ref), (gx_ref, gg_ref), _,
         moves) = _split_refs(refs, 9, 2, exchange)
        i = pl.program_id(0)
        if moves:
            moves.start(also=(i == 0))

        @pl.when(i == 0)
        def _():
            gg_ref[...] = jnp.zeros_like(gg_ref)

        dh = _dot_nt(da_ref[...].astype(BF16), wa_ref[...])
        dh += _dot_nt(db_ref[...].astype(BF16), wb_ref[...])
        dh += _dot_nt(dg_ref[...].astype(BF16), wg_ref[...])
        xv = x_ref[...]
        r = lax.rsqrt(jnp.mean(xv * xv, axis=-1, keepdims=True) + RMS_EPS)
        xh = xv * r
        gg_ref[...] += _colsum(dh * xh)
        dxh = dh * g_ref[...]
        gx_ref[...] = dx2_ref[...] + r * (dxh - xh * jnp.mean(dxh * xh, axis=-1, keepdims=True))
        if moves:
            moves.wait(also=(i == nb - 1))

    row = lambda w: pl.BlockSpec((tm, w), lambda i: (i, 0))
    full = lambda a: pl.BlockSpec(a.shape, lambda i: (0, 0))
    ex_in = exchange.operands if exchange else []
    ex_out = exchange.out_shapes if exchange else []
    res = pl.pallas_call(
        body, name="inproj_bwd", grid=(nb,),
        in_specs=[row(SEC), row(SEC), row(GATE_COLS), full(w_a), full(w_b), full(w_g), row(d), row(d), full(g)]
                 + [ANY] * len(ex_in),
        out_specs=[row(d), pl.BlockSpec((1, d), lambda i: (0, 0))] + [ANY] * len(ex_out),
        out_shape=[jax.ShapeDtypeStruct((s, d), F32), jax.ShapeDtypeStruct((1, d), F32)] + ex_out,
        scratch_shapes=exchange.scratch() if exchange else [],
        compiler_params=_params("arbitrary"),
    )(du_a, du_b, du_g, w_a, w_b, w_g, x, dx2, g, *ex_in)
    return res[0], res[1], list(res[2:])


def _rwkv_elementwise(ua, prev_row, first, mu, wl, w0, a0, kkw, kaw, bd):
    tm = ua.shape[0]
    rows = lax.broadcasted_iota(jnp.int32, (tm, 1), 0)
    prev = jnp.where(first, jnp.zeros_like(prev_row), prev_row)
    shifted = jnp.where(rows == 0, prev, pltpu.roll(ua, 1, 0))
    delta = shifted - ua
    us = ua + delta * mu
    r = us[:, 0:512]
    k0 = us[:, 512:1024]
    v = us[:, 1024:1536]
    lo = us[:, 1536:1664]
    gate = us[:, 1664:2176]
    lane = lax.broadcasted_iota(jnp.int32, (1, LANES), 1)
    th = jnp.tanh(lo)
    lin = jnp.where(lane < LORA, th, lo)
    ll = jnp.dot(lin.astype(BF16), wl, preferred_element_type=F32)
    sz = _sigmoid(w0 + ll[:, :512])
    e = sz * math.exp(-0.5)
    dec = jnp.exp(-e)
    a = _sigmoid(a0 + ll[:, 512:])
    kk0 = k0 * kkw
    ss = _head_sum(kk0 * kk0, bd)
    nrm = jnp.maximum(jnp.sqrt(ss), 1e-12)
    kk = kk0 / nrm
    k = k0 * (1.0 + (a - 1.0) * kaw)
    return dict(delta=delta, us=us, r=r, k0=k0, v=v, lo=lo, gate=gate, th=th, lin=lin, sz=sz, e=e, dec=dec,
                a=a, kk0=kk0, ss=ss, nrm=nrm, kk=kk, k=k)


def _rwkv_prep(u_a, mu, wl, w0, a0, kkw, kaw, tm=256):
    s = u_a.shape[0]

    def body(ua_ref, prev_ref, mu_ref, wl_ref, w0_ref, a0_ref, kkw_ref, kaw_ref,
             r_ref, w_ref, k_ref, v_ref, a_ref, b_ref, g_ref):
        i = pl.program_id(0)
        f = _rwkv_elementwise(ua_ref[...], prev_ref[7:8, :], i == 0, mu_ref[...], wl_ref[...], w0_ref[...],
                              a0_ref[...], kkw_ref[...], kaw_ref[...], _head_ones())
        r_ref[...] = f["r"]
        w_ref[...] = f["dec"]
        k_ref[...] = f["k"]
        v_ref[...] = f["v"]
        a_ref[...] = -f["kk"]
        b_ref[...] = f["kk"] * f["a"]
        g_ref[...] = f["gate"]

    vec = lambda w: pl.BlockSpec((1, w), lambda i: (0, 0))
    out = pl.BlockSpec((tm, D_HALF), lambda i: (i, 0))
    return pl.pallas_call(
        body, name="rwkv_prep", grid=(s // tm,),
        in_specs=[pl.BlockSpec((tm, SEC), lambda i: (i, 0)),
                  pl.BlockSpec((8, SEC), lambda i: (jnp.maximum(i * (tm // 8) - 1, 0), 0)),
                  vec(SEC), pl.BlockSpec((LANES, 2 * D_HALF), lambda i: (0, 0)),
                  vec(D_HALF), vec(D_HALF), vec(D_HALF), vec(D_HALF)],
        out_specs=[out] * 7,
        out_shape=[jax.ShapeDtypeStruct((s, D_HALF), F32)] * 7,
        compiler_params=_params("parallel"),
    )(u_a, u_a, mu, wl, w0, a0, kkw, kaw)


SCAN_TB = 128
N_PAIRS = 4


def _pair_sum(x, left):
    s_l = jnp.sum(jnp.where(left, x, 0.0), axis=1, keepdims=True)
    s_r = jnp.sum(jnp.where(left, 0.0, x), axis=1, keepdims=True)
    return jnp.where(left, s_l, s_r)


def _quad_consts():
    lane = lax.broadcasted_iota(jnp.int32, (HEAD, 2 * LANES), 1)
    rowi = lax.broadcasted_iota(jnp.int32, (HEAD, 2 * LANES), 0)
    diag2 = rowi == (lane & (HEAD - 1))
    r = lax.broadcasted_iota(jnp.int32, (2 * LANES, 2 * LANES), 0) >> 6
    c = lax.broadcasted_iota(jnp.int32, (2 * LANES, 2 * LANES), 1) >> 6
    return diag2, (r == c).astype(BF16)


def _rows_to_columns(x8, diag2, bd2):
    hi = x8.astype(BF16).astype(F32)
    lo = x8 - hi
    lhs = jnp.concatenate([jnp.where(diag2, piece[i:i + 1], 0.0).astype(BF16)
                           for piece in (hi, lo) for i in range(SUBLANES)], axis=0)
    res = jnp.dot(lhs, bd2, preferred_element_type=F32)
    half = SUBLANES * HEAD
    return res[:half] + res[half:]


def _diag_rows(qtile, diag2, bd2, sub_row2):
    half = SUBLANES * HEAD
    res = jnp.dot(qtile, bd2, preferred_element_type=F32)
    out = jnp.zeros((SUBLANES, 2 * LANES), F32)
    for i in range(SUBLANES):
        t = res[i * HEAD:(i + 1) * HEAD] + res[half + i * HEAD:half + (i + 1) * HEAD]
        out = jnp.where(sub_row2 == i, _colsum(jnp.where(diag2, t, 0.0)), out)
    return out


def _store_pieces(qbuf, slot, p, i, x):
    half = SUBLANES * HEAD
    inner = slice((p % 2) * LANES, (p % 2 + 1) * LANES)
    hi = x.astype(BF16)
    qbuf[slot, p // 2, i * HEAD:(i + 1) * HEAD, inner] = hi
    qbuf[slot, p // 2, half + i * HEAD:half + (i + 1) * HEAD, inner] = (x - hi.astype(F32)).astype(BF16)


def _left_half():
    return lax.broadcasted_iota(jnp.int32, (HEAD, LANES), 1) < HEAD


def _split_refs(refs, n_rows, n_out, exchange):
    n_in = len(exchange.operands) if exchange else 0
    n_ex_out = len(exchange.out_shapes) if exchange else 0
    refs = list(refs)
    rows, refs = refs[:n_rows], refs[n_rows:]
    ex_in, refs = refs[:n_in], refs[n_in:]
    outs, refs = refs[:n_out], refs[n_out:]
    ex_out, refs = refs[:n_ex_out], refs[n_ex_out:]
    scratch, sems = (refs[:-3], refs[-3:]) if exchange else (refs, None)
    moves = exchange.moves(ex_in, ex_out, sems) if exchange else None
    return rows, outs, scratch, moves


def _wkv_fwd(r, w, k, a, b, v, exchange=None):
    s = r.shape[0]
    tb = SCAN_TB
    nb = s // tb

    def body(*refs):
        (r_ref, w_ref, k_ref, a_ref, b_ref, v_ref), (y_ref, st_ref), (state, vbuf, qbuf), moves = _split_refs(
            refs, 6, 2, exchange)
        g = pl.program_id(0)
        if moves:
            moves.start(also=(g == 0))

        @pl.when(g == 0)
        def _():
            state[...] = jnp.zeros_like(state)
            qbuf[...] = jnp.zeros_like(qbuf)

        left = _left_half()
        diag2, bd2 = _quad_consts()
        sub_row2 = lax.broadcasted_iota(jnp.int32, (SUBLANES, 2 * LANES), 0)
        groups = tb // SUBLANES
        quads = [slice(g2 * 2 * LANES, (g2 + 1) * 2 * LANES) for g2 in range(2)]

        def rows_of(q):
            return pl.ds(pl.multiple_of(q * SUBLANES, SUBLANES), SUBLANES)

        def v_tiles(q, slot):
            v8 = v_ref[rows_of(q), :]
            for g2 in range(2):
                vbuf[slot, g2] = _rows_to_columns(v8[:, quads[g2]], diag2, bd2)

        def chain(q, slot):
            rows8 = rows_of(q)
            a8, w8, b8, k8, r8 = (x[rows8, :] for x in (a_ref, w_ref, b_ref, k_ref, r_ref))
            sp = [state[p] for p in range(N_PAIRS)]
            for i in range(SUBLANES):
                row = slice(i, i + 1)
                sa = [_pair_sum(sp[p] * a8[row, p * LANES:(p + 1) * LANES], left) for p in range(N_PAIRS)]
                for p in range(N_PAIRS):
                    lanes = slice(p * LANES, (p + 1) * LANES)
                    vt = vbuf[slot, p // 2, i * HEAD:(i + 1) * HEAD, (p % 2) * LANES:(p % 2 + 1) * LANES]
                    sp[p] = sp[p] * w8[row, lanes] + sa[p] * b8[row, lanes] + vt * k8[row, lanes]
                    st_ref[q * SUBLANES + i, p] = sp[p]
                    _store_pieces(qbuf, slot, p, i, sp[p] * r8[row, lanes])
            for p in range(N_PAIRS):
                state[p] = sp[p]

        def y_rows(q, slot):
            for g2 in range(2):
                y_ref[rows_of(q), quads[g2]] = _diag_rows(qbuf[slot, g2], diag2, bd2, sub_row2)

        v_tiles(0, 0)

        def two_groups(j, carry):
            q0 = 2 * j
            v_tiles(q0 + 1, 1)
            chain(q0, 0)
            y_rows(jnp.maximum(q0 - 1, 0), 1)
            v_tiles(jnp.minimum(q0 + 2, groups - 1), 0)
            chain(q0 + 1, 1)
            y_rows(q0, 0)
            return carry

        lax.fori_loop(0, groups // 2, two_groups, 0)
        y_rows(groups - 1, 1)
        if moves:
            moves.wait(also=(g == nb - 1))

    rows = pl.BlockSpec((tb, D_HALF), lambda g: (g, 0))
    ex_in = exchange.operands if exchange else []
    ex_out = exchange.out_shapes if exchange else []
    res = pl.pallas_call(
        body, name="wkv_fwd", grid=(nb,),
        in_specs=[rows] * 6 + [ANY] * len(ex_in),
        out_specs=[rows, pl.BlockSpec((tb, N_PAIRS, HEAD, LANES), lambda g: (g, 0, 0, 0))] + [ANY] * len(ex_out),
        out_shape=[jax.ShapeDtypeStruct((s, D_HALF), F32),
                   jax.ShapeDtypeStruct((s, N_PAIRS, HEAD, LANES), F32)] + ex_out,
        scratch_shapes=[pltpu.VMEM((N_PAIRS, HEAD, LANES), F32),
                        pltpu.VMEM((2, 2, SUBLANES * HEAD, 2 * LANES), F32),
                        pltpu.VMEM((2, 2, 2 * SUBLANES * HEAD, 2 * LANES), BF16)]
                       + (exchange.scratch() if exchange else []),
        compiler_params=_params("arbitrary"),
    )(r, w, k, a, b, v, *ex_in)
    return res[0], res[1], list(res[2:])


def _wkv_bwd(r, w, k, a, b, v, dy, st, exchange=None):
    s = r.shape[0]
    tb = SCAN_TB
    nb = s // tb

    def body(*refs):
        ((r_ref, w_ref, k_ref, a_ref, b_ref, v_ref, dy_ref, st_ref, before_ref),
         (dr_ref, dw_ref, dk_ref, dv_ref, da_ref, db_ref), (dstate, vbuf, qbuf), moves) = _split_refs(refs, 9, 6, exchange)
        g = pl.program_id(0)
        first_block = g == nb - 1
        if moves:
            moves.start(also=(g == 0))

        @pl.when(g == 0)
        def _():
            dstate[...] = jnp.zeros_like(dstate)
            qbuf[...] = jnp.zeros_like(qbuf)

        left = _left_half()
        diag2, bd2 = _quad_consts()
        sub_row = lax.broadcasted_iota(jnp.int32, (SUBLANES, LANES), 0)
        sub_row2 = lax.broadcasted_iota(jnp.int32, (SUBLANES, 2 * LANES), 0)
        groups = tb // SUBLANES
        quads = [slice(g2 * 2 * LANES, (g2 + 1) * 2 * LANES) for g2 in range(2)]
        row_refs = (dr_ref, dw_ref, dk_ref, da_ref, db_ref)

        def rows_of(q):
            return pl.ds(pl.multiple_of(q * SUBLANES, SUBLANES), SUBLANES)

        def column_tiles(q, slot):
            rows8 = rows_of(q)
            for kind, ref in enumerate((v_ref, dy_ref)):
                x8 = ref[rows8, :]
                for g2 in range(2):
                    vbuf[slot, kind, g2] = _rows_to_columns(x8[:, quads[g2]], diag2, bd2)

        def chain(q, slot):
            rows8 = rows_of(q)
            a8, w8, b8, k8, r8 = (x[rows8, :] for x in (a_ref, w_ref, b_ref, k_ref, r_ref))
            dsp = [dstate[p] for p in range(N_PAIRS)]
            outs = [[jnp.zeros((SUBLANES, LANES), F32) for _ in row_refs] for _ in range(N_PAIRS)]
            after = [st_ref[q * SUBLANES + SUBLANES - 1, p] for p in range(N_PAIRS)]
            for i in reversed(range(SUBLANES)):
                row = slice(i, i + 1)
                pl_ = [slice(p * LANES, (p + 1) * LANES) for p in range(N_PAIRS)]
                tile = [(p // 2, slice(i * HEAD, (i + 1) * HEAD), slice((p % 2) * LANES, (p % 2 + 1) * LANES))
                        for p in range(N_PAIRS)]
                if i > 0:
                    sp = [st_ref[q * SUBLANES + i - 1, p] for p in range(N_PAIRS)]
                else:
                    sp = [jnp.where(q == 0, jnp.where(first_block, 0.0, before_ref[0, p]),
                                    st_ref[jnp.maximum(q * SUBLANES - 1, 0), p]) for p in range(N_PAIRS)]
                dyt = [vbuf[(slot, 1) + tile[p]] for p in range(N_PAIRS)]
                ds = [dsp[p] + dyt[p] * r8[row, pl_[p]] for p in range(N_PAIRS)]
                dsa = [_pair_sum(ds[p] * b8[row, pl_[p]], left) for p in range(N_PAIRS)]
                sa = [_pair_sum(sp[p] * a8[row, pl_[p]], left) for p in range(N_PAIRS)]
                for p in range(N_PAIRS):
                    ar, wr, br, kr = (x[row, pl_[p]] for x in (a8, w8, b8, k8))
                    vt = vbuf[(slot, 0) + tile[p]]
                    dsp[p] = ds[p] * wr + dsa[p] * ar
                    new = (_colsum(after[p] * dyt[p]), _colsum(ds[p] * sp[p]), _colsum(ds[p] * vt),
                           _colsum(sp[p] * dsa[p]), _colsum(ds[p] * sa[p]))
                    outs[p] = [jnp.where(sub_row == i, n, o) for n, o in zip(new, outs[p])]
                    _store_pieces(qbuf, slot, p, i, ds[p] * kr)
                after = sp
            for p in range(N_PAIRS):
                dstate[p] = dsp[p]
                for ref, o in zip(row_refs, outs[p]):
                    ref[rows8, p * LANES:(p + 1) * LANES] = o

        def dv_rows(q, slot):
            for g2 in range(2):
                dv_ref[rows_of(q), quads[g2]] = _diag_rows(qbuf[slot, g2], diag2, bd2, sub_row2)

        column_tiles(groups - 1, 0)

        def two_groups(j, carry):
            q0 = groups - 1 - 2 * j
            column_tiles(q0 - 1, 1)
            chain(q0, 0)
            dv_rows(jnp.minimum(q0 + 1, groups - 1), 1)
            column_tiles(jnp.maximum(q0 - 2, 0), 0)
            chain(q0 - 1, 1)
            dv_rows(q0, 0)
            return carry

        lax.fori_loop(0, groups // 2, two_groups, 0)
        dv_rows(0, 1)
        if moves:
            moves.wait(also=(g == nb - 1))

    rows = pl.BlockSpec((tb, D_HALF), lambda g: (nb - 1 - g, 0))
    ex_in = exchange.operands if exchange else []
    ex_out = exchange.out_shapes if exchange else []
    res = pl.pallas_call(
        body, name="wkv_bwd", grid=(nb,),
        in_specs=[rows] * 7 + [pl.BlockSpec((tb, N_PAIRS, HEAD, LANES), lambda g: (nb - 1 - g, 0, 0, 0)),
                               pl.BlockSpec((1, N_PAIRS, HEAD, LANES),
                                            lambda g: (jnp.maximum((nb - 1 - g) * tb - 1, 0), 0, 0, 0))]
                 + [ANY] * len(ex_in),
        out_specs=[rows] * 6 + [ANY] * len(ex_out),
        out_shape=[jax.ShapeDtypeStruct((s, D_HALF), F32)] * 6 + ex_out,
        scratch_shapes=[pltpu.VMEM((N_PAIRS, HEAD, LANES), F32),
                        pltpu.VMEM((2, 2, 2, SUBLANES * HEAD, 2 * LANES), F32),
                        pltpu.VMEM((2, 2, 2 * SUBLANES * HEAD, 2 * LANES), BF16)]
                       + (exchange.scratch() if exchange else []),
        compiler_params=_params("arbitrary"),
    )(r, w, k, a, b, v, dy, st, st, *ex_in)
    return list(res[:6]), list(res[6:])


def _rwkv_post_math(y, r, k, v, gate, lw, lb, rk, bd):
    mean = _head_sum(y, bd) * (1.0 / HEAD)
    yc = y - mean
    var = _head_sum(yc * yc, bd) * (1.0 / HEAD)
    rstd = lax.rsqrt(var + LNX_EPS)
    yn = yc * rstd
    rkk = _head_sum(r * k * rk, bd)
    sg = _sigmoid(gate)
    pre = yn * lw + lb + rkk * v
    return yn, rstd, rkk, sg, pre


def _rwkv_post(y, r, k, v, gate, lw, lb, rk, tm=256):
    s = y.shape[0]

    def body(y_ref, r_ref, k_ref, v_ref, g_ref, lw_ref, lb_ref, rk_ref, o_ref):
        gate_v = g_ref[...]
        _, _, _, sg, pre = _rwkv_post_math(y_ref[...], r_ref[...], k_ref[...], v_ref[...], gate_v,
                                           lw_ref[...], lb_ref[...], rk_ref[...], _head_ones())
        o_ref[...] = pre * (gate_v * sg)

    blk = pl.BlockSpec((tm, D_HALF), lambda i: (i, 0))
    vec = pl.BlockSpec((1, D_HALF), lambda i: (0, 0))
    return pl.pallas_call(
        body, name="rwkv_post", grid=(s // tm,),
        in_specs=[blk] * 5 + [vec] * 3, out_specs=blk,
        out_shape=jax.ShapeDtypeStruct((s, D_HALF), F32), compiler_params=_params("parallel"),
    )(y, r, k, v, gate, lw, lb, rk)


def _rwkv_post_bwd(dmix, y, r, k, v, gate, lw, lb, rk, tm=256):
    s = y.shape[0]

    def body(dm_ref, y_ref, r_ref, k_ref, v_ref, g_ref, lw_ref, lb_ref, rk_ref,
             dy_ref, dr_ref, dk_ref, dv_ref, dg_ref, dlw_ref, dlb_ref, drk_ref):
        i = pl.program_id(0)

        @pl.when(i == 0)
        def _():
            dlw_ref[...] = jnp.zeros_like(dlw_ref)
            dlb_ref[...] = jnp.zeros_like(dlb_ref)
            drk_ref[...] = jnp.zeros_like(drk_ref)

        bd = _head_ones()
        rv, kv, vv, gate_v, lw_v, rk_v = r_ref[...], k_ref[...], v_ref[...], g_ref[...], lw_ref[...], rk_ref[...]
        yn, rstd, rkk, sg, pre = _rwkv_post_math(y_ref[...], rv, kv, vv, gate_v, lw_v, lb_ref[...], rk_v, bd)
        dm = dm_ref[...]
        dg_ref[...] = dm * pre * (sg * (1.0 + gate_v * (1.0 - sg)))
        dpre = dm * (gate_v * sg)
        dlw_ref[...] += _colsum(dpre * yn)
        dlb_ref[...] += _colsum(dpre)
        dyn = dpre * lw_v
        m1 = _head_sum(dyn, bd) * (1.0 / HEAD)
        m2 = _head_sum(dyn * yn, bd) * (1.0 / HEAD)
        dy_ref[...] = rstd * (dyn - m1 - yn * m2)
        dv_ref[...] = dpre * rkk
        drkk = _head_sum(dpre * vv, bd)
        dr_ref[...] = drkk * kv * rk_v
        dk_ref[...] = drkk * rv * rk_v
        drk_ref[...] += _colsum(drkk * rv * kv)

    blk = pl.BlockSpec((tm, D_HALF), lambda i: (i, 0))
    vec = pl.BlockSpec((1, D_HALF), lambda i: (0, 0))
    return pl.pallas_call(
        body, name="rwkv_post_bwd", grid=(s // tm,),
        in_specs=[blk] * 6 + [vec] * 3, out_specs=[blk] * 5 + [vec] * 3,
        out_shape=[jax.ShapeDtypeStruct((s, D_HALF), F32)] * 5 + [jax.ShapeDtypeStruct((1, D_HALF), F32)] * 3,
        compiler_params=_params("arbitrary"),
    )(dmix, y, r, k, v, gate, lw, lb, rk)


def _rwkv_prep_bwd(u_a, grads, mu, wl, w0, a0, kkw, kaw, tm=256):
    s = u_a.shape[0]
    nb = s // tm

    def body(ua_ref, prev_ref, drs_ref, dws_ref, dks_ref, dvs_ref, das_ref, dbs_ref, drb_ref, dkb_ref, dvb_ref,
             dgt_ref, mu_ref, wl_ref, w0_ref, a0_ref, kkw_ref, kaw_ref,
             du_ref, dmu_ref, dwl_ref, dw0_ref, da0_ref, dkkw_ref, dkaw_ref, carry):
        i = pl.program_id(0)

        @pl.when(i == 0)
        def _():
            carry[...] = jnp.zeros_like(carry)
            for ref in (dmu_ref, dwl_ref, dw0_ref, da0_ref, dkkw_ref, dkaw_ref):
                ref[...] = jnp.zeros_like(ref)

        bd = _head_ones()
        mu_v, wl_v, kkw_v, kaw_v = mu_ref[...], wl_ref[...], kkw_ref[...], kaw_ref[...]
        f = _rwkv_elementwise(ua_ref[...], prev_ref[7:8, :], i == nb - 1, mu_v, wl_v, w0_ref[...],
                              a0_ref[...], kkw_v, kaw_v, bd)
        a, kk, k0 = f["a"], f["kk"], f["k0"]
        dk = dks_ref[...] + dkb_ref[...]
        dbs = dbs_ref[...]
        dkk = dbs * a - das_ref[...]
        da = dbs * kk + dk * k0 * kaw_v
        dk0 = dk * (1.0 + (a - 1.0) * kaw_v)
        dkaw_ref[...] += _colsum(dk * k0 * (a - 1.0))
        inv = 1.0 / f["nrm"]
        proj = _head_sum(dkk * kk, bd)
        dkk0 = jnp.where(f["ss"] > 1e-24, (dkk - kk * proj) * inv, dkk * inv)
        dk0 = dk0 + dkk0 * kkw_v
        dkkw_ref[...] += _colsum(dkk0 * k0)
        dza = da * a * (1.0 - a)
        da0_ref[...] += _colsum(dza)
        dz = -dws_ref[...] * f["dec"] * f["e"] * (1.0 - f["sz"])
        dw0_ref[...] += _colsum(dz)
        dll = jnp.concatenate([dz, dza], axis=1).astype(BF16)
        dwl_ref[...] += _dot_tn(f["lin"].astype(BF16), dll)
        dlin = _dot_nt(dll, wl_v)
        lane = lax.broadcasted_iota(jnp.int32, (1, LANES), 1)
        th = f["th"]
        dlo = jnp.where(lane < LORA, dlin * (1.0 - th * th), dlin)
        dus = jnp.concatenate([drs_ref[...] + drb_ref[...], dk0, dvs_ref[...] + dvb_ref[...], dlo, dgt_ref[...]],
                              axis=1)
        dmu_ref[...] += _colsum(dus * f["delta"])
        g1 = dus * mu_v
        rows = lax.broadcasted_iota(jnp.int32, (tm, 1), 0)
        up = jnp.where(rows == tm - 1, carry[...], pltpu.roll(g1, tm - 1, 0))
        du_ref[...] = dus - g1 + up
        carry[...] = g1[0:1, :]

    rev = lambda w: pl.BlockSpec((tm, w), lambda i: (nb - 1 - i, 0))
    vec = lambda w: pl.BlockSpec((1, w), lambda i: (0, 0))
    wl_spec = pl.BlockSpec((LANES, 2 * D_HALF), lambda i: (0, 0))
    return pl.pallas_call(
        body, name="rwkv_prep_bwd", grid=(nb,),
        in_specs=[rev(SEC), pl.BlockSpec((8, SEC), lambda i: (jnp.maximum((nb - 1 - i) * (tm // 8) - 1, 0), 0))]
                 + [rev(D_HALF)] * 10 + [vec(SEC), wl_spec] + [vec(D_HALF)] * 4,
        out_specs=[rev(SEC), vec(SEC), wl_spec] + [vec(D_HALF)] * 4,
        out_shape=[jax.ShapeDtypeStruct((s, SEC), F32), jax.ShapeDtypeStruct((1, SEC), F32),
                   jax.ShapeDtypeStruct((LANES, 2 * D_HALF), F32)] + [jax.ShapeDtypeStruct((1, D_HALF), F32)] * 4,
        scratch_shapes=[pltpu.VMEM((1, SEC), F32)],
        compiler_params=_params("arbitrary"),
    )(u_a, u_a, *grads, mu, wl, w0, a0, kkw, kaw)


def _tri(tm, lower):
    r = lax.broadcasted_iota(jnp.int32, (tm, tm), 0)
    c = lax.broadcasted_iota(jnp.int32, (tm, tm), 1)
    return ((r >= c) if lower else (r <= c)).astype(BF16)


def _head_rms(x, g, bd):
    rinv = lax.rsqrt(_head_sum(x * x, bd) * (1.0 / HEAD) + RMS_EPS)
    xh = x * rinv
    return xh, rinv, xh * g


def _fox_prep(u_b, fb, qg, kg, tm=256):
    s = u_b.shape[0]

    def body(ub_ref, fb_ref, qg_ref, kg_ref, q_ref, k_ref, v_ref, cc_ref, cr_ref, carry):
        i = pl.program_id(0)

        @pl.when(i == 0)
        def _():
            carry[...] = jnp.zeros_like(carry)

        bd = _head_ones()
        _, _, qn = _head_rms(ub_ref[:, 0:512], qg_ref[...], bd)
        _, _, kn = _head_rms(ub_ref[:, 512:1024], kg_ref[...], bd)
        q_ref[...] = (qn * ATT_SCALE).astype(BF16)
        k_ref[...] = kn.astype(BF16)
        v_ref[...] = ub_ref[:, 1024:1536].astype(BF16)
        lane = lax.broadcasted_iota(jnp.int32, (1, LANES), 1)
        logf = jnp.where(lane < N_HEADS, _log_sigmoid(ub_ref[:, 2048:2176] + fb_ref[...]), 0.0)
        cum = _exact_dot(logf, _tri(tm, True), ones_first=True) + carry[...]
        for h in range(N_HEADS):
            cc_ref[h] = jnp.broadcast_to(cum[:, h:h + 1], (tm, LANES))
        cr_ref[...] = jnp.transpose(cum)[0:N_HEADS, :]
        carry[...] = cum[tm - 1:tm, :]

    blk = pl.BlockSpec((tm, D_HALF), lambda i: (i, 0))
    return pl.pallas_call(
        body, name="fox_prep", grid=(s // tm,),
        in_specs=[pl.BlockSpec((tm, SEC), lambda i: (i, 0)), pl.BlockSpec((1, LANES), lambda i: (0, 0)),
                  pl.BlockSpec((1, D_HALF), lambda i: (0, 0)), pl.BlockSpec((1, D_HALF), lambda i: (0, 0))],
        out_specs=[blk, blk, blk, pl.BlockSpec((N_HEADS, tm, LANES), lambda i: (0, i, 0)),
                   pl.BlockSpec((N_HEADS, tm), lambda i: (0, i))],
        out_shape=[jax.ShapeDtypeStruct((s, D_HALF), BF16)] * 3
                  + [jax.ShapeDtypeStruct((N_HEADS, s, LANES), F32), jax.ShapeDtypeStruct((N_HEADS, s), F32)],
        scratch_shapes=[pltpu.VMEM((1, LANES), F32)],
        compiler_params=_params("arbitrary"),
    )(u_b, fb, qg, kg)


ATT_T = 256


def _attn_fwd(q, k, v, cc, cr, u_b):
    s = q.shape[0]
    t = ATT_T
    nblk = s // t

    def body(q_ref, k_ref, v_ref, cc_ref, cr_ref, g_ref, o_ref, mix_ref, lse_ref, m_sc, l_sc, acc_sc):
        i = pl.program_id(0)
        j = pl.program_id(1)

        @pl.when(j == 0)
        def _():
            m_sc[...] = jnp.full_like(m_sc, NEG)
            l_sc[...] = jnp.zeros_like(l_sc)
            acc_sc[...] = jnp.zeros_like(acc_sc)

        @pl.when(j <= i)
        def _():
            row = i * t + lax.broadcasted_iota(jnp.int32, (t, t), 0)
            col = j * t + lax.broadcasted_iota(jnp.int32, (t, t), 1)
            causal = row >= col
            left = lax.broadcasted_iota(jnp.int32, (1, LANES), 1) < HEAD
            for p in range(N_PAIRS):
                lanes = slice(p * LANES, (p + 1) * LANES)
                q2, k2, v2 = q_ref[:, lanes], k_ref[:, lanes], v_ref[:, lanes]
                acc2 = acc_sc[:, lanes]
                for e in range(2):
                    h = 2 * p + e
                    msk = left if e == 0 else jnp.logical_not(left)
                    sc = _dot_nt(jnp.where(msk, q2, jnp.zeros_like(q2)), k2)
                    sc = sc + (_wide(cc_ref[h]) - cr_ref[h:h + 1, :])
                    sc = jnp.where(causal, sc, NEG)
                    m_prev = m_sc[h]
                    m_new = jnp.maximum(m_prev, jnp.max(sc, axis=1, keepdims=True))
                    alpha = jnp.exp(m_prev - m_new)
                    pm = jnp.exp(sc - _wide(m_new))
                    l_sc[h] = alpha * l_sc[h] + jnp.sum(pm, axis=1, keepdims=True)
                    m_sc[h] = m_new
                    pv = jnp.dot(pm.astype(BF16), v2, preferred_element_type=F32)
                    acc2 = jnp.where(msk, alpha * acc2 + pv, acc2)
                acc_sc[:, lanes] = acc2

        @pl.when(j == i)
        def _():
            left = lax.broadcasted_iota(jnp.int32, (1, LANES), 1) < HEAD
            for p in range(N_PAIRS):
                lanes = slice(p * LANES, (p + 1) * LANES)
                inv = jnp.where(left, 1.0 / l_sc[2 * p], 1.0 / l_sc[2 * p + 1])
                o = acc_sc[:, lanes] * inv
                o_ref[:, lanes] = o
                gate = g_ref[:, lanes]
                mix_ref[:, lanes] = o * (gate * _sigmoid(gate))
            for h in range(N_HEADS):
                lse_ref[h] = m_sc[h] + jnp.log(l_sc[h])

    qblk = pl.BlockSpec((t, D_HALF), lambda i, j: (i, 0))
    kblk = pl.BlockSpec((t, D_HALF), lambda i, j: (jnp.minimum(i, j), 0))
    return pl.pallas_call(
        body, name="fox_attn_fwd", grid=(nblk, nblk),
        in_specs=[qblk, kblk, kblk, pl.BlockSpec((N_HEADS, t, LANES), lambda i, j: (0, i, 0)),
                  pl.BlockSpec((N_HEADS, t), lambda i, j: (0, jnp.minimum(i, j))),
                  pl.BlockSpec((t, D_HALF), lambda i, j: (i, 3))],
        out_specs=[qblk, qblk, pl.BlockSpec((N_HEADS, t, LANES), lambda i, j: (0, i, 0))],
        out_shape=[jax.ShapeDtypeStruct((s, D_HALF), F32), jax.ShapeDtypeStruct((s, D_HALF), F32),
                   jax.ShapeDtypeStruct((N_HEADS, s, LANES), F32)],
        scratch_shapes=[pltpu.VMEM((N_HEADS, t, LANES), F32), pltpu.VMEM((N_HEADS, t, LANES), F32),
                        pltpu.VMEM((t, D_HALF), F32)],
        compiler_params=_params("parallel", "arbitrary"),
    )(q, k, v, cc, cr, u_b)


def _fox_post_bwd(dmix, o, u_b, tm=256):
    s = o.shape[0]

    def body(dm_ref, o_ref, g_ref, do_ref, dg_ref):
        gate = g_ref[...]
        sg = _sigmoid(gate)
        dm = dm_ref[...]
        do_ref[...] = (dm * (gate * sg)).astype(BF16)
        dg_ref[...] = dm * o_ref[...] * (sg * (1.0 + gate * (1.0 - sg)))

    blk = pl.BlockSpec((tm, D_HALF), lambda i: (i, 0))
    return pl.pallas_call(
        body, name="fox_post_bwd", grid=(s // tm,),
        in_specs=[blk, blk, pl.BlockSpec((tm, D_HALF), lambda i: (i, 3))], out_specs=[blk] * 2,
        out_shape=[jax.ShapeDtypeStruct((s, D_HALF), BF16), jax.ShapeDtypeStruct((s, D_HALF), F32)],
        compiler_params=_params("parallel"),
    )(dmix, o, u_b)


def _wide(x):
    return jnp.concatenate([x, x], axis=1)


def _attn_probs(q2, k2, v2, do2, msk, causal, bias, lse_rows):
    zero = jnp.zeros_like(q2)
    qh = jnp.where(msk, q2, zero)
    doh = jnp.where(msk, do2, zero)
    sc = jnp.where(causal, _dot_nt(qh, k2) + bias, NEG)
    pm = jnp.exp(sc - _wide(lse_rows))
    dp = _dot_nt(doh, v2)
    return qh, doh, pm, dp


def _attn_bwd_rowdot(q, k, v, do, lse, cc, cr):
    s = q.shape[0]
    t = ATT_T
    nblk = s // t

    def body(q_ref, k_ref, v_ref, do_ref, lse_ref, cc_ref, cr_ref, dd_ref, acc):
        i = pl.program_id(0)
        j = pl.program_id(1)

        @pl.when(j == 0)
        def _():
            acc[...] = jnp.zeros_like(acc)

        @pl.when(j <= i)
        def _():
            row = i * t + lax.broadcasted_iota(jnp.int32, (t, t), 0)
            col = j * t + lax.broadcasted_iota(jnp.int32, (t, t), 1)
            causal = row >= col
            left = lax.broadcasted_iota(jnp.int32, (1, LANES), 1) < HEAD
            for p in range(N_PAIRS):
                lanes = slice(p * LANES, (p + 1) * LANES)
                q2, k2, v2, do2 = q_ref[:, lanes], k_ref[:, lanes], v_ref[:, lanes], do_ref[:, lanes]
                for e in range(2):
                    h = 2 * p + e
                    msk = left if e == 0 else jnp.logical_not(left)
                    bias = _wide(cc_ref[h]) - cr_ref[h:h + 1, :]
                    _, _, pm, dp = _attn_probs(q2, k2, v2, do2, msk, causal, bias, lse_ref[h])
                    acc[h] += jnp.sum(pm * dp, axis=1, keepdims=True)

        @pl.when(j == i)
        def _():
            dd_ref[...] = acc[...]

    qblk = pl.BlockSpec((t, D_HALF), lambda i, j: (i, 0))
    qcol = pl.BlockSpec((N_HEADS, t, LANES), lambda i, j: (0, i, 0))
    kblk = pl.BlockSpec((t, D_HALF), lambda i, j: (jnp.minimum(i, j), 0))
    return pl.pallas_call(
        body, name="fox_attn_rowdot", grid=(nblk, nblk),
        in_specs=[qblk, kblk, kblk, qblk, qcol, qcol, pl.BlockSpec((N_HEADS, t), lambda i, j: (0, jnp.minimum(i, j)))],
        out_specs=qcol, out_shape=jax.ShapeDtypeStruct((N_HEADS, s, LANES), F32),
        scratch_shapes=[pltpu.VMEM((N_HEADS, t, LANES), F32)],
        compiler_params=_params("parallel", "arbitrary"),
    )(q, k, v, do, lse, cc, cr)


def _attn_bwd(q, k, v, do, lse, dd, cc, cr):
    s = q.shape[0]
    t = ATT_T
    nblk = s // t

    def body(q_ref, k_ref, v_ref, do_ref, lse_ref, dd_ref, cc_ref, cr_ref,
             dq_ref, dk_ref, dv_ref, dcr_ref, dk_sc, dv_sc, dcr_sc):
        j = pl.program_id(0)
        i = pl.program_id(1)

        @pl.when(jnp.logical_and(j == 0, i == 0))
        def _():
            dq_ref[...] = jnp.zeros_like(dq_ref)

        @pl.when(i == 0)
        def _():
            dk_sc[...] = jnp.zeros_like(dk_sc)
            dv_sc[...] = jnp.zeros_like(dv_sc)
            dcr_sc[...] = jnp.zeros_like(dcr_sc)

        @pl.when(i >= j)
        def _():
            row = i * t + lax.broadcasted_iota(jnp.int32, (t, t), 0)
            col = j * t + lax.broadcasted_iota(jnp.int32, (t, t), 1)
            causal = row >= col
            left = lax.broadcasted_iota(jnp.int32, (1, LANES), 1) < HEAD
            qrows = pl.ds(pl.multiple_of(i * t, t), t)
            for p in range(N_PAIRS):
                lanes = slice(p * LANES, (p + 1) * LANES)
                q2, k2, v2, do2 = q_ref[:, lanes], k_ref[:, lanes], v_ref[:, lanes], do_ref[:, lanes]
                zero = jnp.zeros_like(q2)
                dq2 = jnp.zeros((t, LANES), F32)
                dk2 = jnp.zeros((t, LANES), F32)
                dv2 = jnp.zeros((t, LANES), F32)
                for e in range(2):
                    h = 2 * p + e
                    msk = left if e == 0 else jnp.logical_not(left)
                    bias = _wide(cc_ref[h]) - cr_ref[h:h + 1, :]
                    qh, doh, pm, dp = _attn_probs(q2, k2, v2, do2, msk, causal, bias, lse_ref[h])
                    dsc = pm * (dp - _wide(dd_ref[h]))
                    dsb = dsc.astype(BF16)
                    dv2 += _dot_tn(pm.astype(BF16), doh)
                    dk2 += _dot_tn(dsb, qh)
                    dq2 += jnp.dot(dsb, jnp.where(msk, k2, zero), preferred_element_type=F32)
                    dcr_sc[h:h + 1, :] += -_colsum(dsc)
                dq_ref[qrows, lanes] += dq2 * ATT_SCALE
                dk_sc[:, lanes] += dk2
                dv_sc[:, lanes] += dv2

        @pl.when(i == nblk - 1)
        def _():
            dk_ref[...] = dk_sc[...]
            dv_ref[...] = dv_sc[...]
            dcr_ref[...] = dcr_sc[...]

    qblk = pl.BlockSpec((t, D_HALF), lambda j, i: (jnp.maximum(i, j), 0))
    qcol = pl.BlockSpec((N_HEADS, t, LANES), lambda j, i: (0, jnp.maximum(i, j), 0))
    kblk = pl.BlockSpec((t, D_HALF), lambda j, i: (j, 0))
    return pl.pallas_call(
        body, name="fox_attn_bwd", grid=(nblk, nblk),
        in_specs=[qblk, kblk, kblk, qblk, qcol, qcol, qcol, pl.BlockSpec((N_HEADS, t), lambda j, i: (0, j))],
        out_specs=[pl.BlockSpec((s, D_HALF), lambda j, i: (0, 0)), kblk, kblk,
                   pl.BlockSpec((N_HEADS, t), lambda j, i: (0, j))],
        out_shape=[jax.ShapeDtypeStruct((s, D_HALF), F32)] * 3 + [jax.ShapeDtypeStruct((N_HEADS, s), F32)],
        scratch_shapes=[pltpu.VMEM((t, D_HALF), F32), pltpu.VMEM((t, D_HALF), F32), pltpu.VMEM((N_HEADS, t), F32)],
        compiler_params=_params("arbitrary", "arbitrary"),
    )(q, k, v, do, lse, dd, cc, cr)


def _fox_prep_bwd(u_b, dq, dk, dv, dgate, dcum, fb, qg, kg, tm=256):
    s = u_b.shape[0]
    nb = s // tm

    def body(ub_ref, dq_ref, dk_ref, dv_ref, dg_ref, dc_ref, fb_ref, qg_ref, kg_ref,
             du_ref, dqg_ref, dkg_ref, dfb_ref, carry):
        i = pl.program_id(0)

        @pl.when(i == 0)
        def _():
            carry[...] = jnp.zeros_like(carry)
            dqg_ref[...] = jnp.zeros_like(dqg_ref)
            dkg_ref[...] = jnp.zeros_like(dkg_ref)
            dfb_ref[...] = jnp.zeros_like(dfb_ref)

        bd = _head_ones()
        for lo, g_ref, d_ref, dgain_ref in ((0, qg_ref, dq_ref, dqg_ref), (512, kg_ref, dk_ref, dkg_ref)):
            gain = g_ref[...]
            xh, rinv, _ = _head_rms(ub_ref[:, lo:lo + 512], gain, bd)
            dn = d_ref[...]
            dgain_ref[...] += _colsum(dn * xh)
            dxh = dn * gain
            du_ref[:, lo:lo + 512] = rinv * (dxh - xh * (_head_sum(dxh * xh, bd) * (1.0 / HEAD)))
        du_ref[:, 1024:1536] = dv_ref[...]
        du_ref[:, 1536:2048] = dg_ref[...]
        lane = lax.broadcasted_iota(jnp.int32, (1, LANES), 1)
        dc = dc_ref[...]
        dlogf = _exact_dot(dc, _tri(tm, False), ones_first=True) + carry[...]
        carry[...] += _colsum(dc)
        fl = ub_ref[:, 2048:2176] + fb_ref[...]
        dfl = jnp.where(lane < N_HEADS, dlogf * (1.0 - _sigmoid(fl)), 0.0)
        du_ref[:, 2048:2176] = dfl
        dfb_ref[...] += _colsum(dfl)

    rev = lambda w: pl.BlockSpec((tm, w), lambda i: (nb - 1 - i, 0))
    vec = lambda w: pl.BlockSpec((1, w), lambda i: (0, 0))
    return pl.pallas_call(
        body, name="fox_prep_bwd", grid=(nb,),
        in_specs=[rev(SEC)] + [rev(D_HALF)] * 4 + [rev(LANES), vec(LANES), vec(D_HALF), vec(D_HALF)],
        out_specs=[rev(SEC), vec(D_HALF), vec(D_HALF), vec(LANES)],
        out_shape=[jax.ShapeDtypeStruct((s, SEC), F32), jax.ShapeDtypeStruct((1, D_HALF), F32),
                   jax.ShapeDtypeStruct((1, D_HALF), F32), jax.ShapeDtypeStruct((1, LANES), F32)],
        scratch_shapes=[pltpu.VMEM((1, LANES), F32)],
        compiler_params=_params("arbitrary"),
    )(u_b, dq, dk, dv, dgate, dcum, fb, qg, kg)


def _merge(mix_a, mix_b, u_g, x, tgt, wa, wb, wo, fg, tm=256):
    s, d = x.shape

    def body(ma_ref, mb_ref, ug_ref, x_ref, t_ref, wa_ref, wb_ref, wo_ref, fg_ref,
             dx2_ref, dma_ref, dmb_ref, dug_ref, dwa_ref, dwb_ref, dwo_ref, dfg_ref, loss_ref):
        i = pl.program_id(0)

        @pl.when(i == 0)
        def _():
            for ref in (dwa_ref, dwb_ref, dwo_ref, dfg_ref, loss_ref):
                ref[...] = jnp.zeros_like(ref)

        wa_v, wb_v, wo_v, fg_v = wa_ref[...], wb_ref[...], wo_ref[...], fg_ref[...]
        ma = ma_ref[...].astype(BF16)
        mb = mb_ref[...].astype(BF16)
        ya = jnp.dot(ma, wa_v, preferred_element_type=F32)
        yb = jnp.dot(mb, wb_v, preferred_element_type=F32)
        sa = _sigmoid(ug_ref[:, 0:d])
        sb = _sigmoid(ug_ref[:, d:2 * d])
        merged = (sa * ya + sb * yb).astype(BF16)
        x2 = x_ref[...] + jnp.dot(merged, wo_v, preferred_element_type=F32)
        r2 = lax.rsqrt(jnp.mean(x2 * x2, axis=-1, keepdims=True) + RMS_EPS)
        x2h = x2 * r2
        err = x2h * fg_v - t_ref[...]
        loss_ref[...] += _colsum(err * err)
        dy = err * (1.0 / d)
        dfg_ref[...] += _colsum(dy * x2h)
        dx2h = dy * fg_v
        dx2 = r2 * (dx2h - x2h * jnp.mean(dx2h * x2h, axis=-1, keepdims=True))
        dx2_ref[...] = dx2
        dx2b = dx2.astype(BF16)
        dmerged = _dot_nt(dx2b, wo_v)
        dwo_ref[...] += _dot_tn(merged, dx2b)
        dya = dmerged * sa
        dyb = dmerged * sb
        dug_ref[:, 0:d] = dya * ya * (1.0 - sa)
        dug_ref[:, d:2 * d] = dyb * yb * (1.0 - sb)
        dyab = dya.astype(BF16)
        dybb = dyb.astype(BF16)
        dma_ref[...] = _dot_nt(dyab, wa_v)
        dmb_ref[...] = _dot_nt(dybb, wb_v)
        dwa_ref[...] += _dot_tn(ma, dyab)
        dwb_ref[...] += _dot_tn(mb, dybb)

    row = lambda w: pl.BlockSpec((tm, w), lambda i: (i, 0))
    full = lambda a: pl.BlockSpec(a.shape, lambda i: (0, 0))
    fshape = lambda a: jax.ShapeDtypeStruct(a.shape, F32)
    return pl.pallas_call(
        body, name="merge_fwd_bwd", grid=(s // tm,),
        in_specs=[row(D_HALF), row(D_HALF), row(GATE_COLS), row(d), row(d), full(wa), full(wb), full(wo), full(fg)],
        out_specs=[row(d), row(D_HALF), row(D_HALF), row(GATE_COLS), full(wa), full(wb), full(wo), full(fg), full(fg)],
        out_shape=[jax.ShapeDtypeStruct((s, d), F32), jax.ShapeDtypeStruct((s, D_HALF), F32),
                   jax.ShapeDtypeStruct((s, D_HALF), F32), jax.ShapeDtypeStruct((s, GATE_COLS), F32),
                   fshape(wa), fshape(wb), fshape(wo), fshape(fg), fshape(fg)],
        compiler_params=_params("arbitrary"),
    )(mix_a, mix_b, u_g, x, tgt, wa, wb, wo, fg)


def _lora_weight(w_up, a_up):
    z = jnp.zeros((LORA, D_HALF), w_up.dtype)
    return jnp.concatenate([jnp.concatenate([w_up, z], axis=1), jnp.concatenate([z, a_up], axis=1)], axis=0)


def _device_grads(x, tgt, p, w_a, w_up, a_up, late_weights, fwd_exchange=None, bwd_exchange=None, tail_exchange=None):
    wl = _lora_weight(w_up, a_up)
    rk = p["r_k"].reshape(1, D_HALF)
    fb = jnp.pad(p["f_bias"], ((0, 0), (0, LANES - N_HEADS)))
    qg = jnp.tile(p["q_norm_g"], (1, N_HEADS))
    kg = jnp.tile(p["k_norm_g"], (1, N_HEADS))
    fg = p["final_norm_g"].reshape(1, D_MODEL)
    mixer = (p["shift_mu"], wl, p["w0"], p["a0"], p["k_k"], p["k_a"])

    h = _rmsnorm_in(x, p["norm_g"])
    u_a = _matmul_nn(h, w_a, "inproj_rwkv")
    r, dec, k, v, av, bv, gate_a = _rwkv_prep(u_a, *mixer)
    y, st, arrived = _wkv_fwd(r, dec, k, av, bv, v, fwd_exchange)
    mix_a = _rwkv_post(y, r, k, v, gate_a, p["lnx_w"], p["lnx_b"], rk)

    w_b, w_g, w_out_a, w_out_b, w_out = late_weights(arrived)
    u_b = _matmul_nn(h, w_b, "inproj_fox")
    u_g = _matmul_nn(h, w_g, "inproj_gate")
    q, kn, vb, cc, cr = _fox_prep(u_b, fb, qg, kg)
    o, mix_b, lse = _attn_fwd(q, kn, vb, cc, cr, u_b)

    dx2, dmix_a, dmix_b, du_g, dwa, dwb, dwo, dfg, loss_vec = _merge(
        mix_a, mix_b, u_g, x, tgt, w_out_a, w_out_b, w_out, fg)

    do, dgate_b = _fox_post_bwd(dmix_b, o, u_b)
    dd = _attn_bwd_rowdot(q, kn, vb, do, lse, cc, cr)
    dq, dk_att, dv_att, dcr = _attn_bwd(q, kn, vb, do, lse, dd, cc, cr)
    dcum = jnp.pad(dcr.T, ((0, 0), (0, LANES - N_HEADS)))
    du_b, dqg, dkg, dfb = _fox_prep_bwd(u_b, dq, dk_att, dv_att, dgate_b, dcum, fb, qg, kg)
    h_t = h.T
    dw_b = _matmul_tn_acc(h_t, du_b, "dw_fox")
    dw_g = _matmul_tn_acc(h_t, du_g, "dw_gate")

    dy, dr_b, dk_b, dv_b, dgate_a, dlw, dlb, drk = _rwkv_post_bwd(
        dmix_a, y, r, k, v, gate_a, p["lnx_w"], p["lnx_b"], rk)
    scan_grads, sent = _wkv_bwd(r, dec, k, av, bv, v, dy, st,
                                bwd_exchange(dw_b, dw_g, dwa, dwb, dwo) if bwd_exchange else None)
    du_a, dmu, dwl, dw0, da0, dkkw, dkaw = _rwkv_prep_bwd(u_a, (*scan_grads, dr_b, dk_b, dv_b, dgate_a), *mixer)
    dw_a = _matmul_tn_acc(h_t, du_a, "dw_rwkv")
    dw_up, da_up = dwl[:LORA, :D_HALF], dwl[LORA:, D_HALF:]
    grad_x, dnorm_g, sent_last = _inproj_bwd(du_a, du_b, du_g, w_a, w_b, w_g, x, dx2, p["norm_g"],
                                            tail_exchange(dw_a, dw_up, da_up) if tail_exchange else None)

    grads = dict(
        norm_g=dnorm_g, w_in=(dw_a, dw_b, dw_g), shift_mu=dmu,
        w_lora_up=dw_up, w0=dw0, a_lora_up=da_up, a0=da0, k_k=dkkw, k_a=dkaw,
        r_k=drk.reshape(1, N_HEADS, HEAD), lnx_w=dlw, lnx_b=dlb, f_bias=dfb[:, :N_HEADS],
        q_norm_g=dqg.reshape(N_HEADS, HEAD).sum(axis=0, keepdims=True),
        k_norm_g=dkg.reshape(N_HEADS, HEAD).sum(axis=0, keepdims=True),
        w_out_a=dwa, w_out_b=dwb, w_out=dwo, final_norm_g=dfg.reshape(D_MODEL))
    return loss_vec, grad_x, grads, sent, sent_last


CHIP_FLIPS = ((1, 0), (0, 1), (1, 1))
ANY = pl.BlockSpec(memory_space=pl.ANY)


def _position():
    return lax.axis_index("x"), lax.axis_index("y"), lax.axis_index("c")


def _flip(v, f):
    return 1 - v if f else v


def _both(a, b):
    if a is None:
        return b
    return a if b is None else jnp.logical_and(a, b)


def _when(cond, fn):
    if cond is None:
        fn()
    else:
        pl.when(cond)(fn)


class _Moves:
    def __init__(self, send_sems, recv_sems, local_sems):
        self.send_sems, self.recv_sems, self.local_sems = send_sems, recv_sems, local_sems
        self.remote, self.local = [], []

    def send(self, src, dst, peer, landing, send_if=None, recv_if=None):
        k = len(self.remote)
        sems = dict(send_sem=self.send_sems.at[k], recv_sem=self.recv_sems.at[k], device_id=peer, device_id_type=MESH)
        out = pltpu.make_async_remote_copy(src_ref=src, dst_ref=dst, **sems)
        arrival = pltpu.make_async_remote_copy(src_ref=src, dst_ref=landing, **sems)
        self.remote.append((out, arrival, send_if, recv_if))

    def copy(self, src, dst, cond=None):
        cp = pltpu.make_async_copy(src, dst, self.local_sems.at[len(self.local)])
        self.local.append((cp, cond))

    def start(self, also=None):
        for cp, cond in self.local:
            _when(_both(also, cond), cp.start)
        for out, _, send_if, _ in self.remote:
            _when(_both(also, send_if), out.start)

    def wait_arrivals(self, also=None):
        for _, arrival, _, recv_if in self.remote:
            _when(_both(also, recv_if), arrival.wait_recv)

    def wait_sent(self, also=None):
        for out, _, send_if, _ in self.remote:
            _when(_both(also, send_if), out.wait_send)
        for cp, cond in self.local:
            _when(_both(also, cond), cp.wait)

    def wait(self, also=None):
        self.wait_arrivals(also)
        self.wait_sent(also)


class _Exchange:
    def __init__(self, operands, out_shapes, n_remote, n_local, build, n_relay=0, relay=None):
        self.operands, self.out_shapes = list(operands), list(out_shapes)
        self.n_remote, self.n_local, self.build = n_remote, n_local, build
        self.n_relay, self.relay = n_relay, relay

    def scratch(self):
        return [pltpu.SemaphoreType.DMA((self.n_remote,)), pltpu.SemaphoreType.DMA((self.n_remote,)),
                pltpu.SemaphoreType.DMA((max(self.n_local, 1),))]

    def moves(self, in_refs, out_refs, sems):
        mv = _Moves(*sems)
        self.build(mv, in_refs, out_refs)
        return mv

    def run_alone(self, name):
        n_in, n_out = len(self.operands), len(self.out_shapes)
        relay_scratch = [pltpu.SemaphoreType.DMA((self.n_relay,))] * 2 if self.relay else []

        def body(*refs):
            ins, outs, sems = refs[:n_in], refs[n_in:n_in + n_out], refs[n_in + n_out:]
            mv = self.moves(ins, outs, sems[:3])
            mv.start()
            mv.wait_arrivals()
            if self.relay:
                passed = _Moves(sems[3], sems[4], None)
                self.relay(passed, ins, outs)
                passed.start()
                passed.wait()
            mv.wait_sent()

        return pl.pallas_call(
            body, name=name, in_specs=[ANY] * n_in, out_specs=[ANY] * n_out, out_shape=self.out_shapes,
            scratch_shapes=self.scratch() + relay_scratch, compiler_params=pltpu.CompilerParams(has_side_effects=True),
        )(*self.operands)


def _is_chip(x, y, chip):
    return jnp.logical_and(x == chip // 2, y == chip % 2)


def _gather_exchange(from_chip, from_all, split=()):
    n1, n2 = len(from_chip), len(from_all)

    def rows_of(t, c):
        half = from_chip[t][1].shape[0] // 2
        return pl.ds(c * half, half)

    def build(mv, ins, outs):
        x, y, c = _position()
        me = 2 * x + y
        for t, (chip, _) in enumerate(from_chip):
            mv.copy(ins[t], outs[t], cond=_is_chip(x, y, chip))
        for t in range(n2):
            mv.copy(ins[n1 + t], outs[n1 + t].at[me])
        for fx, fy in CHIP_FLIPS:
            px, py = _flip(x, fx), _flip(y, fy)
            peer = (px, py, c)
            for t, (chip, _) in enumerate(from_chip):
                part = rows_of(t, c) if t in split else slice(None)
                mv.send(ins[t].at[part], outs[t].at[part], peer, landing=outs[t].at[part],
                        send_if=_is_chip(x, y, chip), recv_if=_is_chip(px, py, chip))
            for t in range(n2):
                mv.send(ins[n1 + t], outs[n1 + t].at[me], peer, landing=outs[n1 + t].at[2 * px + py])

    def relay(mv, ins, outs):
        x, y, c = _position()
        for t in split:
            came = jnp.logical_not(_is_chip(x, y, from_chip[t][0]))
            mv.send(outs[t].at[rows_of(t, c)], outs[t].at[rows_of(t, c)], (x, y, 1 - c),
                    landing=outs[t].at[rows_of(t, 1 - c)], send_if=came, recv_if=came)

    arrays = [a for _, a in from_chip] + list(from_all)
    shapes = [jax.ShapeDtypeStruct(a.shape, a.dtype) for _, a in from_chip]
    shapes += [jax.ShapeDtypeStruct((N_CHIPS,) + a.shape, a.dtype) for a in from_all]
    return _Exchange(arrays, shapes, len(CHIP_FLIPS) * (n1 + n2), n1 + n2, build,
                     n_relay=len(split), relay=relay if split else None)


def _scatter_exchange(to_chip, to_all):
    n1, n2 = len(to_chip), len(to_all)

    def build(mv, ins, outs):
        x, y, c = _position()
        for f, (fx, fy) in enumerate(CHIP_FLIPS):
            px, py = _flip(x, fx), _flip(y, fy)
            peer = (px, py, c)
            for t, (chip, _) in enumerate(to_chip):
                mv.send(ins[t], outs[t].at[f], peer, landing=outs[t].at[f],
                        send_if=_is_chip(px, py, chip), recv_if=_is_chip(x, y, chip))
            for t in range(n2):
                mv.send(ins[n1 + t].at[2 * px + py], outs[n1 + t].at[f], peer, landing=outs[n1 + t].at[f])

    arrays = [a for _, a in to_chip] + list(to_all)
    shapes = [jax.ShapeDtypeStruct((len(CHIP_FLIPS),) + a.shape, a.dtype) for _, a in to_chip]
    shapes += [jax.ShapeDtypeStruct((len(CHIP_FLIPS),) + a.shape[1:], a.dtype) for a in to_all]
    return _Exchange(arrays, shapes, len(CHIP_FLIPS) * (n1 + n2), 0, build)


def _swap_sibling(tensors):
    n = len(tensors)

    def body(*refs):
        ins, outs = refs[:n], refs[n:2 * n]
        send_sems, recv_sems = refs[2 * n:]
        x, y, c = _position()
        copies = [pltpu.make_async_remote_copy(
            src_ref=ins[t], dst_ref=outs[t], send_sem=send_sems.at[t], recv_sem=recv_sems.at[t],
            device_id=(x, y, 1 - c), device_id_type=MESH) for t in range(n)]
        for cp in copies:
            cp.start()
        for cp in copies:
            cp.wait_recv()
        for cp in copies:
            cp.wait_send()

    return pl.pallas_call(
        body, name="swap_sibling", in_specs=[ANY] * n, out_specs=[ANY] * n,
        out_shape=[jax.ShapeDtypeStruct(a.shape, a.dtype) for a in tensors],
        scratch_shapes=[pltpu.SemaphoreType.DMA((n,)), pltpu.SemaphoreType.DMA((n,))],
        compiler_params=pltpu.CompilerParams(has_side_effects=True),
    )(*tensors)


def _allreduce_small(slab):
    stages = 3

    def body(x_ref, o_ref, buf, send_sems, recv_sems):
        x, y, c = _position()
        peers = ((1 - x, y, c), (x, 1 - y, c), (x, y, 1 - c))
        o_ref[...] = x_ref[...]
        for k, peer in enumerate(peers):
            cp = pltpu.make_async_remote_copy(src_ref=o_ref, dst_ref=buf.at[k], send_sem=send_sems.at[k],
                                              recv_sem=recv_sems.at[k], device_id=peer, device_id_type=MESH)
            cp.start()
            cp.wait()
            o_ref[...] = o_ref[...] + buf[k]

    return pl.pallas_call(
        body, name="allreduce_small",
        in_specs=[pl.BlockSpec(memory_space=pltpu.VMEM)], out_specs=pl.BlockSpec(memory_space=pltpu.VMEM),
        out_shape=jax.ShapeDtypeStruct(slab.shape, slab.dtype),
        scratch_shapes=[pltpu.VMEM((stages,) + slab.shape, slab.dtype),
                        pltpu.SemaphoreType.DMA((stages,)), pltpu.SemaphoreType.DMA((stages,))],
        compiler_params=pltpu.CompilerParams(has_side_effects=True),
    )(slab)


def _row_tile(r):
    return min(r, 256)


def _sum4(stack, recv, me):
    _, r, c = stack.shape
    tr = _row_tile(r)

    def body(me_ref, own_ref, recv_ref, o_ref):
        o_ref[...] = (((own_ref[...] + recv_ref[0].astype(F32)) + recv_ref[1].astype(F32))
                      + recv_ref[2].astype(F32))

    return pl.pallas_call(
        body, name="sum_partials",
        grid_spec=pltpu.PrefetchScalarGridSpec(
            num_scalar_prefetch=1, grid=(r // tr,),
            in_specs=[pl.BlockSpec((None, tr, c), lambda i, me_ref: (me_ref[0], i, 0)),
                      pl.BlockSpec((len(CHIP_FLIPS), tr, c), lambda i, me_ref: (0, i, 0))],
            out_specs=pl.BlockSpec((tr, c), lambda i, me_ref: (i, 0))),
        out_shape=jax.ShapeDtypeStruct((r, c), F32), compiler_params=_params("parallel"),
    )(me, stack, recv)


def _sum_block(own, recv):
    r, c = own.shape
    tr = _row_tile(r)

    def body(own_ref, recv_ref, o_ref):
        o_ref[...] = (((own_ref[...] + recv_ref[0].astype(F32)) + recv_ref[1].astype(F32))
                      + recv_ref[2].astype(F32))

    return pl.pallas_call(
        body, name="sum_block", grid=(r // tr,),
        in_specs=[pl.BlockSpec((tr, c), lambda i: (i, 0)), pl.BlockSpec((len(CHIP_FLIPS), tr, c), lambda i: (0, i, 0))],
        out_specs=pl.BlockSpec((tr, c), lambda i: (i, 0)),
        out_shape=jax.ShapeDtypeStruct((r, c), F32), compiler_params=_params("parallel"),
    )(own, recv)


def _adamw_math(w, g, m, v):
    m = ADAM_B1 * m + (1.0 - ADAM_B1) * g
    v = ADAM_B2 * v + (1.0 - ADAM_B2) * (g * g)
    m_hat = m / (1.0 - ADAM_B1 ** ADAM_STEP)
    v_hat = v / (1.0 - ADAM_B2 ** ADAM_STEP)
    delta = -ADAM_LR * (m_hat / (jnp.sqrt(v_hat) + ADAM_EPS) + ADAM_WD * w)
    return delta, m, v


def _adamw(w, m, v, g_parts, name):
    r, c = w.shape
    tr = _row_tile(r)
    n = len(g_parts)

    def body(*refs):
        w_ref, m_ref, v_ref = refs[:3]
        g_refs = refs[3:3 + n]
        g_out, d_out, m_out, v_out = refs[3 + n:]
        g = g_refs[0][...]
        for ref in g_refs[1:]:
            g = g + ref[...]
        g_out[...] = g
        d_out[...], m_out[...], v_out[...] = _adamw_math(w_ref[...], g, m_ref[...], v_ref[...])

    blk = pl.BlockSpec((tr, c), lambda i: (i, 0))
    return pl.pallas_call(
        body, name=name, grid=(r // tr,), in_specs=[blk] * (3 + n), out_specs=[blk] * 4,
        out_shape=[jax.ShapeDtypeStruct((r, c), F32)] * 4, compiler_params=_params("parallel"),
    )(w, m, v, *g_parts)


SHARDED = ("w_in", "w_lora_up", "a_lora_up", "w_out_a", "w_out_b", "w_out")
ROW_SHARDED = ("w_out",)
SMALL = ("norm_g", "shift_mu", "w0", "a0", "k_k", "k_a", "r_k", "lnx_w", "lnx_b", "f_bias", "q_norm_g", "k_norm_g",
         "final_norm_g")
WEIGHTS = ("norm_g", "w_in", "shift_mu", "w_lora_up", "w0", "a_lora_up", "a0", "k_k", "k_a", "r_k", "lnx_w", "lnx_b",
           "f_bias", "q_norm_g", "k_norm_g", "w_out_a", "w_out_b", "w_out", "final_norm_g")
SLAB_ROWS = 16
SLAB_COLS = SEC


def _to_slab(named, extra=None):
    rows = [jnp.pad(named[n].reshape(1, -1), ((0, 0), (0, SLAB_COLS - named[n].size))) for n in SMALL]
    if extra is not None:
        rows.append(jnp.pad(extra.reshape(1, -1), ((0, 0), (0, SLAB_COLS - extra.size))))
    rows.append(jnp.zeros((SLAB_ROWS - len(rows), SLAB_COLS), F32))
    return jnp.concatenate(rows, axis=0)


def _from_slab(slab, shapes):
    return {n: slab[i, :math.prod(shapes[n])].reshape(shapes[n]) for i, n in enumerate(SMALL)}


def _by_chip(g, name):
    if name in ROW_SHARDED:
        return g.reshape(N_CHIPS, g.shape[0] // N_CHIPS, g.shape[1])
    r, c = g.shape
    return g.reshape(r, N_CHIPS, c // N_CHIPS).transpose(1, 0, 2)


def _from_chips(stack, name):
    if name in ROW_SHARDED:
        return stack.reshape(-1, stack.shape[2])
    _, r, c = stack.shape
    return stack.transpose(1, 0, 2).reshape(r, N_CHIPS * c)


def kernel(x, norm_g, w_in, shift_mu, w_lora_up, w0, a_lora_up, a0, k_k, k_a, r_k, lnx_w, lnx_b, f_bias, q_norm_g, k_norm_g, w_out_a, w_out_b, w_out, final_norm_g, loss_target, m_norm_g, m_w_in, m_shift_mu, m_w_lora_up, m_w0, m_a_lora_up, m_a0, m_k_k, m_k_a, m_r_k, m_lnx_w, m_lnx_b, m_f_bias, m_q_norm_g, m_k_norm_g, m_w_out_a, m_w_out_b, m_w_out, m_final_norm_g, v_norm_g, v_w_in, v_shift_mu, v_w_lora_up, v_w0, v_a_lora_up, v_a0, v_k_k, v_k_a, v_r_k, v_lnx_w, v_lnx_b, v_f_bias, v_q_norm_g, v_k_norm_g, v_w_out_a, v_w_out_b, v_w_out, v_final_norm_g):
    w = dict(norm_g=norm_g, w_in=w_in, shift_mu=shift_mu, w_lora_up=w_lora_up, w0=w0, a_lora_up=a_lora_up, a0=a0,
             k_k=k_k, k_a=k_a, r_k=r_k, lnx_w=lnx_w, lnx_b=lnx_b, f_bias=f_bias, q_norm_g=q_norm_g,
             k_norm_g=k_norm_g, w_out_a=w_out_a, w_out_b=w_out_b, w_out=w_out, final_norm_g=final_norm_g)
    m = dict(norm_g=m_norm_g, w_in=m_w_in, shift_mu=m_shift_mu, w_lora_up=m_w_lora_up, w0=m_w0,
             a_lora_up=m_a_lora_up, a0=m_a0, k_k=m_k_k, k_a=m_k_a, r_k=m_r_k, lnx_w=m_lnx_w, lnx_b=m_lnx_b,
             f_bias=m_f_bias, q_norm_g=m_q_norm_g, k_norm_g=m_k_norm_g, w_out_a=m_w_out_a, w_out_b=m_w_out_b,
             w_out=m_w_out, final_norm_g=m_final_norm_g)
    v = dict(norm_g=v_norm_g, w_in=v_w_in, shift_mu=v_shift_mu, w_lora_up=v_w_lora_up, w0=v_w0,
             a_lora_up=v_a_lora_up, a0=v_a0, k_k=v_k_k, k_a=v_k_a, r_k=v_r_k, lnx_w=v_lnx_w, lnx_b=v_lnx_b,
             f_bias=v_f_bias, q_norm_g=v_q_norm_g, k_norm_g=v_k_norm_g, w_out_a=v_w_out_a, w_out_b=v_w_out_b,
             w_out=v_w_out, final_norm_g=v_final_norm_g)
    shapes = {n: w[n].shape for n in WEIGHTS}

    shard = {n: w[n][0].astype(BF16) for n in SHARDED}
    late = ("w_out_a", "w_out_b", "w_out")
    loras = ("w_lora_up", "a_lora_up")
    w_in_head, w_in_tail = shard["w_in"][:, :A_TAIL], shard["w_in"][:, A_TAIL:]
    shard0, shard1_head, up_stack, aup_stack = _gather_exchange(
        [(0, shard["w_in"]), (1, w_in_head)], [shard[n] for n in loras], split=(0,)).run_alone("gather_early")
    w_a = jnp.concatenate([shard0, shard1_head], axis=1)

    def late_weights(arrived):
        shard1_tail, shard2, shard3 = arrived[:3]
        w_b = jnp.concatenate([shard1_tail, shard2[:, :B_TAIL], jnp.zeros((D_MODEL, SEC - FOX_REAL), BF16)], axis=1)
        w_g = jnp.concatenate([shard2[:, B_TAIL:], shard3], axis=1)
        return (w_b, w_g, *[_from_chips(s, n) for n, s in zip(late, arrived[3:])])

    own = {}

    def bwd_exchange(dw_b, dw_g, dwa, dwb, dwo):
        own["tail1"] = dw_b[:, :B_HEAD]
        own["block2"] = jnp.concatenate([dw_b[:, B_HEAD:FOX_REAL], dw_g[:, :G_HEAD]], axis=1)
        own["block3"] = dw_g[:, G_HEAD:]
        own.update({n: _by_chip(g, n) for n, g in zip(late, (dwa, dwb, dwo))})
        return _scatter_exchange([(1, own["tail1"].astype(BF16)), (2, own["block2"].astype(BF16)),
                                  (3, own["block3"].astype(BF16))], [own[n].astype(BF16) for n in late])

    def tail_exchange(dw_a, dw_up, da_up):
        own["block0"], own["head1"] = dw_a[:, :SHARD_COLS], dw_a[:, SHARD_COLS:]
        own.update({n: _by_chip(g, n) for n, g in zip(loras, (dw_up, da_up))})
        return _scatter_exchange([(0, own["block0"].astype(BF16)), (1, own["head1"].astype(BF16))],
                                 [own[n].astype(BF16) for n in loras])

    small = {n: w[n] for n in SMALL}
    loss_vec, grad_x, grads, sent, sent_last = _device_grads(
        x[0], loss_target[0], small, w_a, _from_chips(up_stack, "w_lora_up"), _from_chips(aup_stack, "a_lora_up"),
        late_weights, _gather_exchange([(1, w_in_tail), (2, shard["w_in"]), (3, shard["w_in"])], [shard[n] for n in late]),
        bwd_exchange, tail_exchange)

    total = _allreduce_small(_to_slab(grads, extra=loss_vec))
    loss = (0.5 / D_MODEL) * jnp.sum(total[len(SMALL)])
    slab_g, slab_d, slab_m, slab_v = _adamw(_to_slab(w), _to_slab(m), _to_slab(v), [total], "adamw_small")
    out_g, out_d, out_m, out_v = (_from_slab(s, shapes) for s in (total, slab_d, slab_m, slab_v))
    del slab_g

    xpos, ypos, _ = _position()
    me = (2 * xpos + ypos).astype(jnp.int32).reshape(1)
    core_sum = {"w_in": lax.switch(me[0], [
        lambda: _sum_block(own["block0"], sent_last[0]),
        lambda: jnp.concatenate([_sum_block(own["head1"], sent_last[1]), _sum_block(own["tail1"], sent[0])], axis=1),
        lambda: _sum_block(own["block2"], sent[1]),
        lambda: _sum_block(own["block3"], sent[2])])}
    core_sum.update({n: _sum4(own[n], r, me) for n, r in zip(late, sent[3:])})
    core_sum.update({n: _sum4(own[n], r, me) for n, r in zip(loras, sent_last[2:])})
    sibling_sums = _swap_sibling([core_sum[n] for n in SHARDED])
    for n, theirs in zip(SHARDED, sibling_sums):
        g, d, m2, v2 = _adamw(w[n][0], m[n][0], v[n][0], [core_sum[n], theirs], "adamw_" + n)
        out_g[n], out_d[n], out_m[n], out_v[n] = (a.reshape(shapes[n]) for a in (g, d, m2, v2))

    return (loss, grad_x.reshape(x.shape), *[out_g[n] for n in WEIGHTS], *[out_d[n] for n in WEIGHTS],
            *[out_m[n] for n in WEIGHTS], *[out_v[n] for n in WEIGHTS])
```

```python
import functools
import math

import jax
import jax.numpy as jnp
from jax import lax
from jax.experimental import pallas as pl
from jax.experimental.pallas import tpu as pltpu

F32 = jnp.float32
BF16 = jnp.bfloat16

D_MODEL = 1024
D_HALF = 512
HEAD = 64
N_HEADS = 8
LORA = 64
RWKV_COLS = 2176
FOX_REAL = 2056
SEC = 2176
GATE_COLS = 2048
IN_COLS = 6280
N_CHIPS = 4
SHARD_COLS = IN_COLS // N_CHIPS
A_TAIL = RWKV_COLS - SHARD_COLS
B_HEAD = SHARD_COLS - A_TAIL
B_TAIL = FOX_REAL - B_HEAD
G_HEAD = SHARD_COLS - B_TAIL
RMS_EPS = 1e-6
LNX_EPS = 64e-5
ATT_SCALE = HEAD ** -0.5
NEG = -1e30

ADAM_LR = 0.001
ADAM_B1 = 0.9
ADAM_B2 = 0.999
ADAM_EPS = 1e-08
ADAM_WD = 0.01
ADAM_STEP = 10

LANES = 128
SUBLANES = 8
VMEM_LIMIT = 56 * 1024 * 1024
MESH = pl.DeviceIdType.MESH


def _params(*sem):
    return pltpu.CompilerParams(dimension_semantics=sem if sem else None, vmem_limit_bytes=VMEM_LIMIT)


def _sigmoid(x):
    return 1.0 / (1.0 + jnp.exp(-x))


def _log_sigmoid(x):
    return jnp.minimum(x, 0.0) - jnp.log(1.0 + jnp.exp(-jnp.abs(x)))


def _head_ones():
    r = lax.broadcasted_iota(jnp.int32, (LANES, LANES), 0) >> 6
    c = lax.broadcasted_iota(jnp.int32, (LANES, LANES), 1) >> 6
    return (r == c).astype(BF16)


def _split3(x):
    hi = x.astype(BF16)
    r1 = x - hi.astype(F32)
    mid = r1.astype(BF16)
    lo = (r1 - mid.astype(F32)).astype(BF16)
    return hi, mid, lo


def _exact_dot(x, ones_bf16, ones_first=False):
    out = None
    for piece in _split3(x):
        if ones_first:
            t = jnp.dot(ones_bf16, piece, preferred_element_type=F32)
        else:
            t = jnp.dot(piece, ones_bf16, preferred_element_type=F32)
        out = t if out is None else out + t
    return out


def _head_sum(x, bd):
    n = x.shape[1] // LANES
    parts = [_exact_dot(x[:, i * LANES:(i + 1) * LANES], bd) for i in range(n)]
    return parts[0] if n == 1 else jnp.concatenate(parts, axis=1)


def _dot_nt(a, b):
    return lax.dot_general(a, b, (((1,), (1,)), ((), ())), preferred_element_type=F32)


def _dot_tn(a, b):
    return lax.dot_general(a, b, (((0,), (0,)), ((), ())), preferred_element_type=F32)


def _colsum(x):
    return jnp.sum(x, axis=0, keepdims=True)


def _rmsnorm_in(x, g, tm=512):
    s, d = x.shape

    def body(x_ref, g_ref, h_ref):
        xv = x_ref[...]
        r = lax.rsqrt(jnp.mean(xv * xv, axis=-1, keepdims=True) + RMS_EPS)
        h_ref[...] = (xv * r * g_ref[...]).astype(BF16)

    return pl.pallas_call(
        body, name="rmsnorm_in", grid=(s // tm,),
        in_specs=[pl.BlockSpec((tm, d), lambda i: (i, 0)), pl.BlockSpec((1, d), lambda i: (0, 0))],
        out_specs=pl.BlockSpec((tm, d), lambda i: (i, 0)),
        out_shape=jax.ShapeDtypeStruct((s, d), BF16), compiler_params=_params("parallel"),
    )(x, g)


def _matmul_nn(a, b, name, tm=512):
    m, k = a.shape
    n = b.shape[1]

    def body(a_ref, b_ref, o_ref):
        o_ref[...] = jnp.dot(a_ref[...], b_ref[...], preferred_element_type=F32)

    return pl.pallas_call(
        body, name=name, grid=(m // tm,),
        in_specs=[pl.BlockSpec((tm, k), lambda i: (i, 0)), pl.BlockSpec((k, n), lambda i: (0, 0))],
        out_specs=pl.BlockSpec((tm, n), lambda i: (i, 0)),
        out_shape=jax.ShapeDtypeStruct((m, n), F32), compiler_params=_params("parallel"),
    )(a, b)


def _matmul_tn_acc(at, b, name, tk=512):
    m, k = at.shape
    n = b.shape[1]

    def body(a_ref, b_ref, o_ref):
        j = pl.program_id(0)

        @pl.when(j == 0)
        def _():
            o_ref[...] = jnp.zeros_like(o_ref)

        o_ref[...] += jnp.dot(a_ref[...], b_ref[...].astype(BF16), preferred_element_type=F32)

    return pl.pallas_call(
        body, name=name, grid=(k // tk,),
        in_specs=[pl.BlockSpec((m, tk), lambda j: (0, j)), pl.BlockSpec((tk, n), lambda j: (j, 0))],
        out_specs=pl.BlockSpec((m, n), lambda j: (0, 0)),
        out_shape=jax.ShapeDtypeStruct((m, n), F32), compiler_params=_params("arbitrary"),
    )(at, b)


def _inproj_bwd(du_a, du_b, du_g, w_a, w_b, w_g, x, dx2, g, exchange=None, tm=256):
    s, d = x.shape
    nb = s // tm

    def body(*refs):
        ((da_ref, db_ref, dg_ref, wa_ref, wb_ref, wg_ref, x_ref, dx2_ref, g_ref), (gx_ref, gg_ref), _,
         moves) = _split_refs(refs, 9, 2, exchange)
        i = pl.program_id(0)
        if moves:
            moves.start(also=(i == 0))

        @pl.when(i == 0)
        def _():
            gg_ref[...] = jnp.zeros_like(gg_ref)

        dh = _dot_nt(da_ref[...].astype(BF16), wa_ref[...])
        dh += _dot_nt(db_ref[...].astype(BF16), wb_ref[...])
        dh += _dot_nt(dg_ref[...].astype(BF16), wg_ref[...])
        xv = x_ref[...]
        r = lax.rsqrt(jnp.mean(xv * xv, axis=-1, keepdims=True) + RMS_EPS)
        xh = xv * r
        gg_ref[...] += _colsum(dh * xh)
        dxh = dh * g_ref[...]
        gx_ref[...] = dx2_ref[...] + r * (dxh - xh * jnp.mean(dxh * xh, axis=-1, keepdims=True))
        if moves:
            moves.wait(also=(i == nb - 1))

    row = lambda w: pl.BlockSpec((tm, w), lambda i: (i, 0))
    full = lambda a: pl.BlockSpec(a.shape, lambda i: (0, 0))
    ex_in = exchange.operands if exchange else []
    ex_out = exchange.out_shapes if exchange else []
    res = pl.pallas_call(
        body, name="inproj_bwd", grid=(nb,),
        in_specs=[row(SEC), row(SEC), row(GATE_COLS), full(w_a), full(w_b), full(w_g), row(d), row(d), full(g)]
                 + [ANY] * len(ex_in),
        out_specs=[row(d), pl.BlockSpec((1, d), lambda i: (0, 0))] + [ANY] * len(ex_out),
        out_shape=[jax.ShapeDtypeStruct((s, d), F32), jax.ShapeDtypeStruct((1, d), F32)] + ex_out,
        scratch_shapes=exchange.scratch() if exchange else [],
        compiler_params=_params("arbitrary"),
    )(du_a, du_b, du_g, w_a, w_b, w_g, x, dx2, g, *ex_in)
    return res[0], res[1], list(res[2:])


def _rwkv_elementwise(ua, prev_row, first, mu, wl, w0, a0, kkw, kaw, bd):
    tm = ua.shape[0]
    rows = lax.broadcasted_iota(jnp.int32, (tm, 1), 0)
    prev = jnp.where(first, jnp.zeros_like(prev_row), prev_row)
    shifted = jnp.where(rows == 0, prev, pltpu.roll(ua, 1, 0))
    delta = shifted - ua
    us = ua + delta * mu
    r = us[:, 0:512]
    k0 = us[:, 512:1024]
    v = us[:, 1024:1536]
    lo = us[:, 1536:1664]
    gate = us[:, 1664:2176]
    lane = lax.broadcasted_iota(jnp.int32, (1, LANES), 1)
    th = jnp.tanh(lo)
    lin = jnp.where(lane < LORA, th, lo)
    ll = jnp.dot(lin.astype(BF16), wl, preferred_element_type=F32)
    sz = _sigmoid(w0 + ll[:, :512])
    e = sz * math.exp(-0.5)
    dec = jnp.exp(-e)
    a = _sigmoid(a0 + ll[:, 512:])
    kk0 = k0 * kkw
    ss = _head_sum(kk0 * kk0, bd)
    nrm = jnp.maximum(jnp.sqrt(ss), 1e-12)
    kk = kk0 / nrm
    k = k0 * (1.0 + (a - 1.0) * kaw)
    return dict(delta=delta, us=us, r=r, k0=k0, v=v, lo=lo, gate=gate, th=th, lin=lin, sz=sz, e=e, dec=dec,
                a=a, kk0=kk0, ss=ss, nrm=nrm, kk=kk, k=k)


def _rwkv_prep(u_a, mu, wl, w0, a0, kkw, kaw, tm=256):
    s = u_a.shape[0]

    def body(ua_ref, prev_ref, mu_ref, wl_ref, w0_ref, a0_ref, kkw_ref, kaw_ref,
             r_ref, w_ref, k_ref, v_ref, a_ref, b_ref, g_ref):
        i = pl.program_id(0)
        f = _rwkv_elementwise(ua_ref[...], prev_ref[7:8, :], i == 0, mu_ref[...], wl_ref[...], w0_ref[...],
                              a0_ref[...], kkw_ref[...], kaw_ref[...], _head_ones())
        r_ref[...] = f["r"]
        w_ref[...] = f["dec"]
        k_ref[...] = f["k"]
        v_ref[...] = f["v"]
        a_ref[...] = -f["kk"]
        b_ref[...] = f["kk"] * f["a"]
        g_ref[...] = f["gate"]

    vec = lambda w: pl.BlockSpec((1, w), lambda i: (0, 0))
    out = pl.BlockSpec((tm, D_HALF), lambda i: (i, 0))
    return pl.pallas_call(
        body, name="rwkv_prep", grid=(s // tm,),
        in_specs=[pl.BlockSpec((tm, SEC), lambda i: (i, 0)),
                  pl.BlockSpec((8, SEC), lambda i: (jnp.maximum(i * (tm // 8) - 1, 0), 0)),
                  vec(SEC), pl.BlockSpec((LANES, 2 * D_HALF), lambda i: (0, 0)),
                  vec(D_HALF), vec(D_HALF), vec(D_HALF), vec(D_HALF)],
        out_specs=[out] * 7,
        out_shape=[jax.ShapeDtypeStruct((s, D_HALF), F32)] * 7,
        compiler_params=_params("parallel"),
    )(u_a, u_a, mu, wl, w0, a0, kkw, kaw)


SCAN_TB = 128
N_PAIRS = 4


def _pair_sum(x, left):
    s_l = jnp.sum(jnp.where(left, x, 0.0), axis=1, keepdims=True)
    s_r = jnp.sum(jnp.where(left, 0.0, x), axis=1, keepdims=True)
    return jnp.where(left, s_l, s_r)


def _quad_consts():
    lane = lax.broadcasted_iota(jnp.int32, (HEAD, 2 * LANES), 1)
    rowi = lax.broadcasted_iota(jnp.int32, (HEAD, 2 * LANES), 0)
    diag2 = rowi == (lane & (HEAD - 1))
    r = lax.broadcasted_iota(jnp.int32, (2 * LANES, 2 * LANES), 0) >> 6
    c = lax.broadcasted_iota(jnp.int32, (2 * LANES, 2 * LANES), 1) >> 6
    return diag2, (r == c).astype(BF16)


def _rows_to_columns(x8, diag2, bd2):
    lhs = jnp.concatenate([jnp.where(diag2, x8[i:i + 1], 0.0).astype(BF16) for i in range(SUBLANES)], axis=0)
    return jnp.dot(lhs, bd2, preferred_element_type=F32)


def _diag_rows(qtile, diag2, bd2, sub_row2):
    res = jnp.dot(qtile, bd2, preferred_element_type=F32)
    out = jnp.zeros((SUBLANES, 2 * LANES), F32)
    for i in range(SUBLANES):
        out = jnp.where(sub_row2 == i, _colsum(jnp.where(diag2, res[i * HEAD:(i + 1) * HEAD], 0.0)), out)
    return out


def _store_tile(qbuf, slot, p, i, x):
    qbuf[slot, p // 2, i * HEAD:(i + 1) * HEAD, (p % 2) * LANES:(p % 2 + 1) * LANES] = x.astype(BF16)


def _left_half():
    return lax.broadcasted_iota(jnp.int32, (HEAD, LANES), 1) < HEAD


def _split_refs(refs, n_rows, n_out, exchange):
    n_in = len(exchange.operands) if exchange else 0
    n_ex_out = len(exchange.out_shapes) if exchange else 0
    refs = list(refs)
    rows, refs = refs[:n_rows], refs[n_rows:]
    ex_in, refs = refs[:n_in], refs[n_in:]
    outs, refs = refs[:n_out], refs[n_out:]
    ex_out, refs = refs[:n_ex_out], refs[n_ex_out:]
    scratch, sems = (refs[:-3], refs[-3:]) if exchange else (refs, None)
    moves = exchange.moves(ex_in, ex_out, sems) if exchange else None
    return rows, outs, scratch, moves


def _wkv_fwd(r, w, k, a, b, v, exchange=None):
    s = r.shape[0]
    tb = SCAN_TB
    nb = s // tb

    def body(*refs):
        (r_ref, w_ref, k_ref, a_ref, b_ref, v_ref), (y_ref, st_ref), (state, vbuf, qbuf), moves = _split_refs(
            refs, 6, 2, exchange)
        g = pl.program_id(0)
        if moves:
            moves.start(also=(g == 0))

        @pl.when(g == 0)
        def _():
            state[...] = jnp.zeros_like(state)
            qbuf[...] = jnp.zeros_like(qbuf)

        left = _left_half()
        diag2, bd2 = _quad_consts()
        sub_row2 = lax.broadcasted_iota(jnp.int32, (SUBLANES, 2 * LANES), 0)
        groups = tb // SUBLANES
        quads = [slice(g2 * 2 * LANES, (g2 + 1) * 2 * LANES) for g2 in range(2)]

        def rows_of(q):
            return pl.ds(pl.multiple_of(q * SUBLANES, SUBLANES), SUBLANES)

        def v_tiles(q, slot):
            v8 = v_ref[rows_of(q), :]
            for g2 in range(2):
                vbuf[slot, g2] = _rows_to_columns(v8[:, quads[g2]], diag2, bd2)

        def chain(q, slot):
            rows8 = rows_of(q)
            a8, w8, b8, k8, r8 = (x[rows8, :] for x in (a_ref, w_ref, b_ref, k_ref, r_ref))
            pairs = [slice(p * LANES, (p + 1) * LANES) for p in range(N_PAIRS)]
            a_next = pltpu.roll(a8, SUBLANES - 1, 0)
            wa8 = w8 * a_next
            ba8 = jnp.concatenate([_pair_sum(b8[:, pr] * a_next[:, pr], left[0:SUBLANES]) for pr in pairs], axis=1)
            ka8 = jnp.concatenate([_pair_sum(k8[:, pr] * a_next[:, pr], left[0:SUBLANES]) for pr in pairs], axis=1)
            sp = [state[p] for p in range(N_PAIRS)]
            for i in range(0, SUBLANES, 2):
                r0, r1 = slice(i, i + 1), slice(i + 1, i + 2)
                sums = [(_pair_sum(sp[p] * a8[r0, pairs[p]], left), _pair_sum(sp[p] * wa8[r0, pairs[p]], left))
                        for p in range(N_PAIRS)]
                sa0, sa1 = [s[0] for s in sums], [s[1] for s in sums]
                for p in range(N_PAIRS):
                    pr = pairs[p]
                    inner = slice((p % 2) * LANES, (p % 2 + 1) * LANES)
                    vt0 = vbuf[slot, p // 2, i * HEAD:(i + 1) * HEAD, inner]
                    vt1 = vbuf[slot, p // 2, (i + 1) * HEAD:(i + 2) * HEAD, inner]
                    sa_next = sa1[p] + sa0[p] * ba8[r0, pr] + vt0 * ka8[r0, pr]
                    s1 = sp[p] * w8[r0, pr] + sa0[p] * b8[r0, pr] + vt0 * k8[r0, pr]
                    st_ref[q * SUBLANES + i, p] = s1
                    _store_tile(qbuf, slot, p, i, s1 * r8[r0, pr])
                    s2 = s1 * w8[r1, pr] + sa_next * b8[r1, pr] + vt1 * k8[r1, pr]
                    st_ref[q * SUBLANES + i + 1, p] = s2
                    _store_tile(qbuf, slot, p, i + 1, s2 * r8[r1, pr])
                    sp[p] = s2
            for p in range(N_PAIRS):
                state[p] = sp[p]

        def y_rows(q, slot):
            for g2 in range(2):
                y_ref[rows_of(q), quads[g2]] = _diag_rows(qbuf[slot, g2], diag2, bd2, sub_row2)

        v_tiles(0, 0)

        def two_groups(j, carry):
            q0 = 2 * j
            v_tiles(q0 + 1, 1)
            chain(q0, 0)
            y_rows(jnp.maximum(q0 - 1, 0), 1)
            v_tiles(jnp.minimum(q0 + 2, groups - 1), 0)
            chain(q0 + 1, 1)
            y_rows(q0, 0)
            return carry

        lax.fori_loop(0, groups // 2, two_groups, 0)
        y_rows(groups - 1, 1)
        if moves:
            moves.wait(also=(g == nb - 1))

    rows = pl.BlockSpec((tb, D_HALF), lambda g: (g, 0))
    ex_in = exchange.operands if exchange else []
    ex_out = exchange.out_shapes if exchange else []
    res = pl.pallas_call(
        body, name="wkv_fwd", grid=(nb,),
        in_specs=[rows] * 6 + [ANY] * len(ex_in),
        out_specs=[rows, pl.BlockSpec((tb, N_PAIRS, HEAD, LANES), lambda g: (g, 0, 0, 0))] + [ANY] * len(ex_out),
        out_shape=[jax.ShapeDtypeStruct((s, D_HALF), F32),
                   jax.ShapeDtypeStruct((s, N_PAIRS, HEAD, LANES), F32)] + ex_out,
        scratch_shapes=[pltpu.VMEM((N_PAIRS, HEAD, LANES), F32),
                        pltpu.VMEM((2, 2, SUBLANES * HEAD, 2 * LANES), F32),
                        pltpu.VMEM((2, 2, SUBLANES * HEAD, 2 * LANES), BF16)]
                       + (exchange.scratch() if exchange else []),
        compiler_params=_params("arbitrary"),
    )(r, w, k, a, b, v, *ex_in)
    return res[0], res[1], list(res[2:])


def _wkv_bwd(r, w, k, a, b, v, dy, st, exchange=None):
    s = r.shape[0]
    tb = SCAN_TB
    nb = s // tb

    def body(*refs):
        ((r_ref, w_ref, k_ref, a_ref, b_ref, v_ref, dy_ref, st_ref, before_ref),
         (dr_ref, dw_ref, dk_ref, dv_ref, da_ref, db_ref), (dstate, vbuf, qbuf), moves) = _split_refs(refs, 9, 6, exchange)
        g = pl.program_id(0)
        first_block = g == nb - 1
        if moves:
            moves.start(also=(g == 0))

        @pl.when(g == 0)
        def _():
            dstate[...] = jnp.zeros_like(dstate)
            qbuf[...] = jnp.zeros_like(qbuf)

        left = _left_half()
        diag2, bd2 = _quad_consts()
        sub_row = lax.broadcasted_iota(jnp.int32, (SUBLANES, LANES), 0)
        sub_row2 = lax.broadcasted_iota(jnp.int32, (SUBLANES, 2 * LANES), 0)
        groups = tb // SUBLANES
        quads = [slice(g2 * 2 * LANES, (g2 + 1) * 2 * LANES) for g2 in range(2)]
        row_refs = (dr_ref, dw_ref, dk_ref, da_ref, db_ref)

        def rows_of(q):
            return pl.ds(pl.multiple_of(q * SUBLANES, SUBLANES), SUBLANES)

        def column_tiles(q, slot):
            rows8 = rows_of(q)
            for kind, ref in enumerate((v_ref, dy_ref)):
                x8 = ref[rows8, :]
                for g2 in range(2):
                    vbuf[slot, kind, g2] = _rows_to_columns(x8[:, quads[g2]], diag2, bd2)

        def chain(q, slot):
            rows8 = rows_of(q)
            a8, w8, b8, k8, r8 = (x[rows8, :] for x in (a_ref, w_ref, b_ref, k_ref, r_ref))
            dsp = [dstate[p] for p in range(N_PAIRS)]
            outs = [[jnp.zeros((SUBLANES, LANES), F32) for _ in row_refs] for _ in range(N_PAIRS)]
            after = [st_ref[q * SUBLANES + SUBLANES - 1, p] for p in range(N_PAIRS)]
            for i in reversed(range(SUBLANES)):
                row = slice(i, i + 1)
                pl_ = [slice(p * LANES, (p + 1) * LANES) for p in range(N_PAIRS)]
                tile = [(p // 2, slice(i * HEAD, (i + 1) * HEAD), slice((p % 2) * LANES, (p % 2 + 1) * LANES))
                        for p in range(N_PAIRS)]
                if i > 0:
                    sp = [st_ref[q * SUBLANES + i - 1, p] for p in range(N_PAIRS)]
                else:
                    sp = [jnp.where(q == 0, jnp.where(first_block, 0.0, before_ref[0, p]),
                                    st_ref[jnp.maximum(q * SUBLANES - 1, 0), p]) for p in range(N_PAIRS)]
                dyt = [vbuf[(slot, 1) + tile[p]] for p in range(N_PAIRS)]
                ds = [dsp[p] + dyt[p] * r8[row, pl_[p]] for p in range(N_PAIRS)]
                dsa = [_pair_sum(ds[p] * b8[row, pl_[p]], left) for p in range(N_PAIRS)]
                sa = [_pair_sum(sp[p] * a8[row, pl_[p]], left) for p in range(N_PAIRS)]
                for p in range(N_PAIRS):
                    ar, wr, br, kr = (x[row, pl_[p]] for x in (a8, w8, b8, k8))
                    vt = vbuf[(slot, 0) + tile[p]]
                    dsp[p] = ds[p] * wr + dsa[p] * ar
                    new = (_colsum(after[p] * dyt[p]), _colsum(ds[p] * sp[p]), _colsum(ds[p] * vt),
                           _colsum(sp[p] * dsa[p]), _colsum(ds[p] * sa[p]))
                    outs[p] = [jnp.where(sub_row == i, n, o) for n, o in zip(new, outs[p])]
                    _store_tile(qbuf, slot, p, i, ds[p] * kr)
                after = sp
            for p in range(N_PAIRS):
                dstate[p] = dsp[p]
                for ref, o in zip(row_refs, outs[p]):
                    ref[rows8, p * LANES:(p + 1) * LANES] = o

        def dv_rows(q, slot):
            for g2 in range(2):
                dv_ref[rows_of(q), quads[g2]] = _diag_rows(qbuf[slot, g2], diag2, bd2, sub_row2)

        column_tiles(groups - 1, 0)

        def two_groups(j, carry):
            q0 = groups - 1 - 2 * j
            column_tiles(q0 - 1, 1)
            chain(q0, 0)
            dv_rows(jnp.minimum(q0 + 1, groups - 1), 1)
            column_tiles(jnp.maximum(q0 - 2, 0), 0)
            chain(q0 - 1, 1)
            dv_rows(q0, 0)
            return carry

        lax.fori_loop(0, groups // 2, two_groups, 0)
        dv_rows(0, 1)
        if moves:
            moves.wait(also=(g == nb - 1))

    rows = pl.BlockSpec((tb, D_HALF), lambda g: (nb - 1 - g, 0))
    ex_in = exchange.operands if exchange else []
    ex_out = exchange.out_shapes if exchange else []
    res = pl.pallas_call(
        body, name="wkv_bwd", grid=(nb,),
        in_specs=[rows] * 7 + [pl.BlockSpec((tb, N_PAIRS, HEAD, LANES), lambda g: (nb - 1 - g, 0, 0, 0)),
                               pl.BlockSpec((1, N_PAIRS, HEAD, LANES),
                                            lambda g: (jnp.maximum((nb - 1 - g) * tb - 1, 0), 0, 0, 0))]
                 + [ANY] * len(ex_in),
        out_specs=[rows] * 6 + [ANY] * len(ex_out),
        out_shape=[jax.ShapeDtypeStruct((s, D_HALF), F32)] * 6 + ex_out,
        scratch_shapes=[pltpu.VMEM((N_PAIRS, HEAD, LANES), F32),
                        pltpu.VMEM((2, 2, 2, SUBLANES * HEAD, 2 * LANES), F32),
                        pltpu.VMEM((2, 2, SUBLANES * HEAD, 2 * LANES), BF16)]
                       + (exchange.scratch() if exchange else []),
        compiler_params=_params("arbitrary"),
    )(r, w, k, a, b, v, dy, st, st, *ex_in)
    return list(res[:6]), list(res[6:])


def _rwkv_post_math(y, r, k, v, gate, lw, lb, rk, bd):
    mean = _head_sum(y, bd) * (1.0 / HEAD)
    yc = y - mean
    var = _head_sum(yc * yc, bd) * (1.0 / HEAD)
    rstd = lax.rsqrt(var + LNX_EPS)
    yn = yc * rstd
    rkk = _head_sum(r * k * rk, bd)
    sg = _sigmoid(gate)
    pre = yn * lw + lb + rkk * v
    return yn, rstd, rkk, sg, pre


def _rwkv_post(y, r, k, v, gate, lw, lb, rk, tm=256):
    s = y.shape[0]

    def body(y_ref, r_ref, k_ref, v_ref, g_ref, lw_ref, lb_ref, rk_ref, o_ref):
        gate_v = g_ref[...]
        _, _, _, sg, pre = _rwkv_post_math(y_ref[...], r_ref[...], k_ref[...], v_ref[...], gate_v,
                                           lw_ref[...], lb_ref[...], rk_ref[...], _head_ones())
        o_ref[...] = pre * (gate_v * sg)

    blk = pl.BlockSpec((tm, D_HALF), lambda i: (i, 0))
    vec = pl.BlockSpec((1, D_HALF), lambda i: (0, 0))
    return pl.pallas_call(
        body, name="rwkv_post", grid=(s // tm,),
        in_specs=[blk] * 5 + [vec] * 3, out_specs=blk,
        out_shape=jax.ShapeDtypeStruct((s, D_HALF), F32), compiler_params=_params("parallel"),
    )(y, r, k, v, gate, lw, lb, rk)


def _rwkv_post_bwd(dmix, y, r, k, v, gate, lw, lb, rk, tm=256):
    s = y.shape[0]

    def body(dm_ref, y_ref, r_ref, k_ref, v_ref, g_ref, lw_ref, lb_ref, rk_ref,
             dy_ref, dr_ref, dk_ref, dv_ref, dg_ref, dlw_ref, dlb_ref, drk_ref):
        i = pl.program_id(0)

        @pl.when(i == 0)
        def _():
            dlw_ref[...] = jnp.zeros_like(dlw_ref)
            dlb_ref[...] = jnp.zeros_like(dlb_ref)
            drk_ref[...] = jnp.zeros_like(drk_ref)

        bd = _head_ones()
        rv, kv, vv, gate_v, lw_v, rk_v = r_ref[...], k_ref[...], v_ref[...], g_ref[...], lw_ref[...], rk_ref[...]
        yn, rstd, rkk, sg, pre = _rwkv_post_math(y_ref[...], rv, kv, vv, gate_v, lw_v, lb_ref[...], rk_v, bd)
        dm = dm_ref[...]
        dg_ref[...] = dm * pre * (sg * (1.0 + gate_v * (1.0 - sg)))
        dpre = dm * (gate_v * sg)
        dlw_ref[...] += _colsum(dpre * yn)
        dlb_ref[...] += _colsum(dpre)
        dyn = dpre * lw_v
        m1 = _head_sum(dyn, bd) * (1.0 / HEAD)
        m2 = _head_sum(dyn * yn, bd) * (1.0 / HEAD)
        dy_ref[...] = rstd * (dyn - m1 - yn * m2)
        dv_ref[...] = dpre * rkk
        drkk = _head_sum(dpre * vv, bd)
        dr_ref[...] = drkk * kv * rk_v
        dk_ref[...] = drkk * rv * rk_v
        drk_ref[...] += _colsum(drkk * rv * kv)

    blk = pl.BlockSpec((tm, D_HALF), lambda i: (i, 0))
    vec = pl.BlockSpec((1, D_HALF), lambda i: (0, 0))
    return pl.pallas_call(
        body, name="rwkv_post_bwd", grid=(s // tm,),
        in_specs=[blk] * 6 + [vec] * 3, out_specs=[blk] * 5 + [vec] * 3,
        out_shape=[jax.ShapeDtypeStruct((s, D_HALF), F32)] * 5 + [jax.ShapeDtypeStruct((1, D_HALF), F32)] * 3,
        compiler_params=_params("arbitrary"),
    )(dmix, y, r, k, v, gate, lw, lb, rk)


def _rwkv_prep_bwd(u_a, grads, mu, wl, w0, a0, kkw, kaw, tm=256):
    s = u_a.shape[0]
    nb = s // tm

    def body(ua_ref, prev_ref, drs_ref, dws_ref, dks_ref, dvs_ref, das_ref, dbs_ref, drb_ref, dkb_ref, dvb_ref,
             dgt_ref, mu_ref, wl_ref, w0_ref, a0_ref, kkw_ref, kaw_ref,
             du_ref, dmu_ref, dwl_ref, dw0_ref, da0_ref, dkkw_ref, dkaw_ref, carry):
        i = pl.program_id(0)

        @pl.when(i == 0)
        def _():
            carry[...] = jnp.zeros_like(carry)
            for ref in (dmu_ref, dwl_ref, dw0_ref, da0_ref, dkkw_ref, dkaw_ref):
                ref[...] = jnp.zeros_like(ref)

        bd = _head_ones()
        mu_v, wl_v, kkw_v, kaw_v = mu_ref[...], wl_ref[...], kkw_ref[...], kaw_ref[...]
        f = _rwkv_elementwise(ua_ref[...], prev_ref[7:8, :], i == nb - 1, mu_v, wl_v, w0_ref[...],
                              a0_ref[...], kkw_v, kaw_v, bd)
        a, kk, k0 = f["a"], f["kk"], f["k0"]
        dk = dks_ref[...] + dkb_ref[...]
        dbs = dbs_ref[...]
        dkk = dbs * a - das_ref[...]
        da = dbs * kk + dk * k0 * kaw_v
        dk0 = dk * (1.0 + (a - 1.0) * kaw_v)
        dkaw_ref[...] += _colsum(dk * k0 * (a - 1.0))
        inv = 1.0 / f["nrm"]
        proj = _head_sum(dkk * kk, bd)
        dkk0 = jnp.where(f["ss"] > 1e-24, (dkk - kk * proj) * inv, dkk * inv)
        dk0 = dk0 + dkk0 * kkw_v
        dkkw_ref[...] += _colsum(dkk0 * k0)
        dza = da * a * (1.0 - a)
        da0_ref[...] += _colsum(dza)
        dz = -dws_ref[...] * f["dec"] * f["e"] * (1.0 - f["sz"])
        dw0_ref[...] += _colsum(dz)
        dll = jnp.concatenate([dz, dza], axis=1).astype(BF16)
        dwl_ref[...] += _dot_tn(f["lin"].astype(BF16), dll)
        dlin = _dot_nt(dll, wl_v)
        lane = lax.broadcasted_iota(jnp.int32, (1, LANES), 1)
        th = f["th"]
        dlo = jnp.where(lane < LORA, dlin * (1.0 - th * th), dlin)
        dus = jnp.concatenate([drs_ref[...] + drb_ref[...], dk0, dvs_ref[...] + dvb_ref[...], dlo, dgt_ref[...]],
                              axis=1)
        dmu_ref[...] += _colsum(dus * f["delta"])
        g1 = dus * mu_v
        rows = lax.broadcasted_iota(jnp.int32, (tm, 1), 0)
        up = jnp.where(rows == tm - 1, carry[...], pltpu.roll(g1, tm - 1, 0))
        du_ref[...] = dus - g1 + up
        carry[...] = g1[0:1, :]

    rev = lambda w: pl.BlockSpec((tm, w), lambda i: (nb - 1 - i, 0))
    vec = lambda w: pl.BlockSpec((1, w), lambda i: (0, 0))
    wl_spec = pl.BlockSpec((LANES, 2 * D_HALF), lambda i: (0, 0))
    return pl.pallas_call(
        body, name="rwkv_prep_bwd", grid=(nb,),
        in_specs=[rev(SEC), pl.BlockSpec((8, SEC), lambda i: (jnp.maximum((nb - 1 - i) * (tm // 8) - 1, 0), 0))]
                 + [rev(D_HALF)] * 10 + [vec(SEC), wl_spec] + [vec(D_HALF)] * 4,
        out_specs=[rev(SEC), vec(SEC), wl_spec] + [vec(D_HALF)] * 4,
        out_shape=[jax.ShapeDtypeStruct((s, SEC), F32), jax.ShapeDtypeStruct((1, SEC), F32),
                   jax.ShapeDtypeStruct((LANES, 2 * D_HALF), F32)] + [jax.ShapeDtypeStruct((1, D_HALF), F32)] * 4,
        scratch_shapes=[pltpu.VMEM((1, SEC), F32)],
        compiler_params=_params("arbitrary"),
    )(u_a, u_a, *grads, mu, wl, w0, a0, kkw, kaw)


def _tri(tm, lower):
    r = lax.broadcasted_iota(jnp.int32, (tm, tm), 0)
    c = lax.broadcasted_iota(jnp.int32, (tm, tm), 1)
    return ((r >= c) if lower else (r <= c)).astype(BF16)


def _head_rms(x, g, bd):
    rinv = lax.rsqrt(_head_sum(x * x, bd) * (1.0 / HEAD) + RMS_EPS)
    xh = x * rinv
    return xh, rinv, xh * g


def _fox_prep(u_b, fb, qg, kg, tm=256):
    s = u_b.shape[0]

    def body(ub_ref, fb_ref, qg_ref, kg_ref, q_ref, k_ref, v_ref, cc_ref, cr_ref, carry):
        i = pl.program_id(0)

        @pl.when(i == 0)
        def _():
            carry[...] = jnp.zeros_like(carry)

        bd = _head_ones()
        _, _, qn = _head_rms(ub_ref[:, 0:512], qg_ref[...], bd)
        _, _, kn = _head_rms(ub_ref[:, 512:1024], kg_ref[...], bd)
        q_ref[...] = (qn * ATT_SCALE).astype(BF16)
        k_ref[...] = kn.astype(BF16)
        v_ref[...] = ub_ref[:, 1024:1536].astype(BF16)
        lane = lax.broadcasted_iota(jnp.int32, (1, LANES), 1)
        logf = jnp.where(lane < N_HEADS, _log_sigmoid(ub_ref[:, 2048:2176] + fb_ref[...]), 0.0)
        cum = _exact_dot(logf, _tri(tm, True), ones_first=True) + carry[...]
        for h in range(N_HEADS):
            cc_ref[h] = jnp.broadcast_to(cum[:, h:h + 1], (tm, LANES))
        cr_ref[...] = jnp.transpose(cum)[0:N_HEADS, :]
        carry[...] = cum[tm - 1:tm, :]

    blk = pl.BlockSpec((tm, D_HALF), lambda i: (i, 0))
    return pl.pallas_call(
        body, name="fox_prep", grid=(s // tm,),
        in_specs=[pl.BlockSpec((tm, SEC), lambda i: (i, 0)), pl.BlockSpec((1, LANES), lambda i: (0, 0)),
                  pl.BlockSpec((1, D_HALF), lambda i: (0, 0)), pl.BlockSpec((1, D_HALF), lambda i: (0, 0))],
        out_specs=[blk, blk, blk, pl.BlockSpec((N_HEADS, tm, LANES), lambda i: (0, i, 0)),
                   pl.BlockSpec((N_HEADS, tm), lambda i: (0, i))],
        out_shape=[jax.ShapeDtypeStruct((s, D_HALF), BF16)] * 3
                  + [jax.ShapeDtypeStruct((N_HEADS, s, LANES), F32), jax.ShapeDtypeStruct((N_HEADS, s), F32)],
        scratch_shapes=[pltpu.VMEM((1, LANES), F32)],
        compiler_params=_params("arbitrary"),
    )(u_b, fb, qg, kg)


ATT_T = 256


def _attn_fwd(q, k, v, cc, cr, u_b):
    s = q.shape[0]
    t = ATT_T
    nblk = s // t

    def body(q_ref, k_ref, v_ref, cc_ref, cr_ref, g_ref, o_ref, mix_ref, lse_ref, m_sc, l_sc, acc_sc):
        i = pl.program_id(0)
        j = pl.program_id(1)

        @pl.when(j == 0)
        def _():
            m_sc[...] = jnp.full_like(m_sc, NEG)
            l_sc[...] = jnp.zeros_like(l_sc)
            acc_sc[...] = jnp.zeros_like(acc_sc)

        @pl.when(j <= i)
        def _():
            row = i * t + lax.broadcasted_iota(jnp.int32, (t, t), 0)
            col = j * t + lax.broadcasted_iota(jnp.int32, (t, t), 1)
            causal = row >= col
            left = lax.broadcasted_iota(jnp.int32, (1, LANES), 1) < HEAD
            for p in range(N_PAIRS):
                lanes = slice(p * LANES, (p + 1) * LANES)
                q2, k2, v2 = q_ref[:, lanes], k_ref[:, lanes], v_ref[:, lanes]
                acc2 = acc_sc[:, lanes]
                for e in range(2):
                    h = 2 * p + e
                    msk = left if e == 0 else jnp.logical_not(left)
                    sc = _dot_nt(jnp.where(msk, q2, jnp.zeros_like(q2)), k2)
                    sc = sc + (_wide(cc_ref[h]) - cr_ref[h:h + 1, :])
                    sc = jnp.where(causal, sc, NEG)
                    m_prev = m_sc[h]
                    m_new = jnp.maximum(m_prev, jnp.max(sc, axis=1, keepdims=True))
                    alpha = jnp.exp(m_prev - m_new)
                    pm = jnp.exp(sc - _wide(m_new))
                    l_sc[h] = alpha * l_sc[h] + jnp.sum(pm, axis=1, keepdims=True)
                    m_sc[h] = m_new
                    pv = jnp.dot(pm.astype(BF16), v2, preferred_element_type=F32)
                    acc2 = jnp.where(msk, alpha * acc2 + pv, acc2)
                acc_sc[:, lanes] = acc2

        @pl.when(j == i)
        def _():
            left = lax.broadcasted_iota(jnp.int32, (1, LANES), 1) < HEAD
            for p in range(N_PAIRS):
                lanes = slice(p * LANES, (p + 1) * LANES)
                inv = jnp.where(left, 1.0 / l_sc[2 * p], 1.0 / l_sc[2 * p + 1])
                o = acc_sc[:, lanes] * inv
                o_ref[:, lanes] = o
                gate = g_ref[:, lanes]
                mix_ref[:, lanes] = o * (gate * _sigmoid(gate))
            for h in range(N_HEADS):
                lse_ref[h] = m_sc[h] + jnp.log(l_sc[h])

    qblk = pl.BlockSpec((t, D_HALF), lambda i, j: (i, 0))
    kblk = pl.BlockSpec((t, D_HALF), lambda i, j: (jnp.minimum(i, j), 0))
    return pl.pallas_call(
        body, name="fox_attn_fwd", grid=(nblk, nblk),
        in_specs=[qblk, kblk, kblk, pl.BlockSpec((N_HEADS, t, LANES), lambda i, j: (0, i, 0)),
                  pl.BlockSpec((N_HEADS, t), lambda i, j: (0, jnp.minimum(i, j))),
                  pl.BlockSpec((t, D_HALF), lambda i, j: (i, 3))],
        out_specs=[qblk, qblk, pl.BlockSpec((N_HEADS, t, LANES), lambda i, j: (0, i, 0))],
        out_shape=[jax.ShapeDtypeStruct((s, D_HALF), F32), jax.ShapeDtypeStruct((s, D_HALF), F32),
                   jax.ShapeDtypeStruct((N_HEADS, s, LANES), F32)],
        scratch_shapes=[pltpu.VMEM((N_HEADS, t, LANES), F32), pltpu.VMEM((N_HEADS, t, LANES), F32),
                        pltpu.VMEM((t, D_HALF), F32)],
        compiler_params=_params("parallel", "arbitrary"),
    )(q, k, v, cc, cr, u_b)


def _fox_post_bwd(dmix, o, u_b, tm=256):
    s = o.shape[0]

    def body(dm_ref, o_ref, g_ref, do_ref, dg_ref):
        gate = g_ref[...]
        sg = _sigmoid(gate)
        dm = dm_ref[...]
        do_ref[...] = (dm * (gate * sg)).astype(BF16)
        dg_ref[...] = dm * o_ref[...] * (sg * (1.0 + gate * (1.0 - sg)))

    blk = pl.BlockSpec((tm, D_HALF), lambda i: (i, 0))
    return pl.pallas_call(
        body, name="fox_post_bwd", grid=(s // tm,),
        in_specs=[blk, blk, pl.BlockSpec((tm, D_HALF), lambda i: (i, 3))], out_specs=[blk] * 2,
        out_shape=[jax.ShapeDtypeStruct((s, D_HALF), BF16), jax.ShapeDtypeStruct((s, D_HALF), F32)],
        compiler_params=_params("parallel"),
    )(dmix, o, u_b)


def _wide(x):
    return jnp.concatenate([x, x], axis=1)


def _attn_probs(q2, k2, v2, do2, msk, causal, bias, lse_rows):
    zero = jnp.zeros_like(q2)
    qh = jnp.where(msk, q2, zero)
    doh = jnp.where(msk, do2, zero)
    sc = jnp.where(causal, _dot_nt(qh, k2) + bias, NEG)
    pm = jnp.exp(sc - _wide(lse_rows))
    dp = _dot_nt(doh, v2)
    return qh, doh, pm, dp


def _attn_bwd_rowdot(q, k, v, do, lse, cc, cr):
    s = q.shape[0]
    t = ATT_T
    nblk = s // t

    def body(q_ref, k_ref, v_ref, do_ref, lse_ref, cc_ref, cr_ref, dd_ref, acc):
        i = pl.program_id(0)
        j = pl.program_id(1)

        @pl.when(j == 0)
        def _():
            acc[...] = jnp.zeros_like(acc)

        @pl.when(j <= i)
        def _():
            row = i * t + lax.broadcasted_iota(jnp.int32, (t, t), 0)
            col = j * t + lax.broadcasted_iota(jnp.int32, (t, t), 1)
            causal = row >= col
            left = lax.broadcasted_iota(jnp.int32, (1, LANES), 1) < HEAD
            for p in range(N_PAIRS):
                lanes = slice(p * LANES, (p + 1) * LANES)
                q2, k2, v2, do2 = q_ref[:, lanes], k_ref[:, lanes], v_ref[:, lanes], do_ref[:, lanes]
                for e in range(2):
                    h = 2 * p + e
                    msk = left if e == 0 else jnp.logical_not(left)
                    bias = _wide(cc_ref[h]) - cr_ref[h:h + 1, :]
                    _, _, pm, dp = _attn_probs(q2, k2, v2, do2, msk, causal, bias, lse_ref[h])
                    acc[h] += jnp.sum(pm * dp, axis=1, keepdims=True)

        @pl.when(j == i)
        def _():
            dd_ref[...] = acc[...]

    qblk = pl.BlockSpec((t, D_HALF), lambda i, j: (i, 0))
    qcol = pl.BlockSpec((N_HEADS, t, LANES), lambda i, j: (0, i, 0))
    kblk = pl.BlockSpec((t, D_HALF), lambda i, j: (jnp.minimum(i, j), 0))
    return pl.pallas_call(
        body, name="fox_attn_rowdot", grid=(nblk, nblk),
        in_specs=[qblk, kblk, kblk, qblk, qcol, qcol, pl.BlockSpec((N_HEADS, t), lambda i, j: (0, jnp.minimum(i, j)))],
        out_specs=qcol, out_shape=jax.ShapeDtypeStruct((N_HEADS, s, LANES), F32),
        scratch_shapes=[pltpu.VMEM((N_HEADS, t, LANES), F32)],
        compiler_params=_params("parallel", "arbitrary"),
    )(q, k, v, do, lse, cc, cr)


def _attn_bwd(q, k, v, do, lse, dd, cc, cr):
    s = q.shape[0]
    t = ATT_T
    nblk = s // t

    def body(q_ref, k_ref, v_ref, do_ref, lse_ref, dd_ref, cc_ref, cr_ref,
             dq_ref, dk_ref, dv_ref, dcr_ref, dk_sc, dv_sc, dcr_sc):
        j = pl.program_id(0)
        i = pl.program_id(1)

        @pl.when(jnp.logical_and(j == 0, i == 0))
        def _():
            dq_ref[...] = jnp.zeros_like(dq_ref)

        @pl.when(i == 0)
        def _():
            dk_sc[...] = jnp.zeros_like(dk_sc)
            dv_sc[...] = jnp.zeros_like(dv_sc)
            dcr_sc[...] = jnp.zeros_like(dcr_sc)

        @pl.when(i >= j)
        def _():
            row = i * t + lax.broadcasted_iota(jnp.int32, (t, t), 0)
            col = j * t + lax.broadcasted_iota(jnp.int32, (t, t), 1)
            causal = row >= col
            left = lax.broadcasted_iota(jnp.int32, (1, LANES), 1) < HEAD
            qrows = pl.ds(pl.multiple_of(i * t, t), t)
            for p in range(N_PAIRS):
                lanes = slice(p * LANES, (p + 1) * LANES)
                q2, k2, v2, do2 = q_ref[:, lanes], k_ref[:, lanes], v_ref[:, lanes], do_ref[:, lanes]
                zero = jnp.zeros_like(q2)
                dq2 = jnp.zeros((t, LANES), F32)
                dk2 = jnp.zeros((t, LANES), F32)
                dv2 = jnp.zeros((t, LANES), F32)
                for e in range(2):
                    h = 2 * p + e
                    msk = left if e == 0 else jnp.logical_not(left)
                    bias = _wide(cc_ref[h]) - cr_ref[h:h + 1, :]
                    qh, doh, pm, dp = _attn_probs(q2, k2, v2, do2, msk, causal, bias, lse_ref[h])
                    dsc = pm * (dp - _wide(dd_ref[h]))
                    dsb = dsc.astype(BF16)
                    dv2 += _dot_tn(pm.astype(BF16), doh)
                    dk2 += _dot_tn(dsb, qh)
                    dq2 += jnp.dot(dsb, jnp.where(msk, k2, zero), preferred_element_type=F32)
                    dcr_sc[h:h + 1, :] += -_colsum(dsc)
                dq_ref[qrows, lanes] += dq2 * ATT_SCALE
                dk_sc[:, lanes] += dk2
                dv_sc[:, lanes] += dv2

        @pl.when(i == nblk - 1)
        def _():
            dk_ref[...] = dk_sc[...]
            dv_ref[...] = dv_sc[...]
            dcr_ref[...] = dcr_sc[...]

    qblk = pl.BlockSpec((t, D_HALF), lambda j, i: (jnp.maximum(i, j), 0))
    qcol = pl.BlockSpec((N_HEADS, t, LANES), lambda j, i: (0, jnp.maximum(i, j), 0))
    kblk = pl.BlockSpec((t, D_HALF), lambda j, i: (j, 0))
    return pl.pallas_call(
        body, name="fox_attn_bwd", grid=(nblk, nblk),
        in_specs=[qblk, kblk, kblk, qblk, qcol, qcol, qcol, pl.BlockSpec((N_HEADS, t), lambda j, i: (0, j))],
        out_specs=[pl.BlockSpec((s, D_HALF), lambda j, i: (0, 0)), kblk, kblk,
                   pl.BlockSpec((N_HEADS, t), lambda j, i: (0, j))],
        out_shape=[jax.ShapeDtypeStruct((s, D_HALF), F32)] * 3 + [jax.ShapeDtypeStruct((N_HEADS, s), F32)],
        scratch_shapes=[pltpu.VMEM((t, D_HALF), F32), pltpu.VMEM((t, D_HALF), F32), pltpu.VMEM((N_HEADS, t), F32)],
        compiler_params=_params("arbitrary", "arbitrary"),
    )(q, k, v, do, lse, dd, cc, cr)


def _fox_prep_bwd(u_b, dq, dk, dv, dgate, dcum, fb, qg, kg, tm=256):
    s = u_b.shape[0]
    nb = s // tm

    def body(ub_ref, dq_ref, dk_ref, dv_ref, dg_ref, dc_ref, fb_ref, qg_ref, kg_ref,
             du_ref, dqg_ref, dkg_ref, dfb_ref, carry):
        i = pl.program_id(0)

        @pl.when(i == 0)
        def _():
            carry[...] = jnp.zeros_like(carry)
            dqg_ref[...] = jnp.zeros_like(dqg_ref)
            dkg_ref[...] = jnp.zeros_like(dkg_ref)
            dfb_ref[...] = jnp.zeros_like(dfb_ref)

        bd = _head_ones()
        for lo, g_ref, d_ref, dgain_ref in ((0, qg_ref, dq_ref, dqg_ref), (512, kg_ref, dk_ref, dkg_ref)):
            gain = g_ref[...]
            xh, rinv, _ = _head_rms(ub_ref[:, lo:lo + 512], gain, bd)
            dn = d_ref[...]
            dgain_ref[...] += _colsum(dn * xh)
            dxh = dn * gain
            du_ref[:, lo:lo + 512] = rinv * (dxh - xh * (_head_sum(dxh * xh, bd) * (1.0 / HEAD)))
        du_ref[:, 1024:1536] = dv_ref[...]
        du_ref[:, 1536:2048] = dg_ref[...]
        lane = lax.broadcasted_iota(jnp.int32, (1, LANES), 1)
        dc = dc_ref[...]
        dlogf = _exact_dot(dc, _tri(tm, False), ones_first=True) + carry[...]
        carry[...] += _colsum(dc)
        fl = ub_ref[:, 2048:2176] + fb_ref[...]
        dfl = jnp.where(lane < N_HEADS, dlogf * (1.0 - _sigmoid(fl)), 0.0)
        du_ref[:, 2048:2176] = dfl
        dfb_ref[...] += _colsum(dfl)

    rev = lambda w: pl.BlockSpec((tm, w), lambda i: (nb - 1 - i, 0))
    vec = lambda w: pl.BlockSpec((1, w), lambda i: (0, 0))
    return pl.pallas_call(
        body, name="fox_prep_bwd", grid=(nb,),
        in_specs=[rev(SEC)] + [rev(D_HALF)] * 4 + [rev(LANES), vec(LANES), vec(D_HALF), vec(D_HALF)],
        out_specs=[rev(SEC), vec(D_HALF), vec(D_HALF), vec(LANES)],
        out_shape=[jax.ShapeDtypeStruct((s, SEC), F32), jax.ShapeDtypeStruct((1, D_HALF), F32),
                   jax.ShapeDtypeStruct((1, D_HALF), F32), jax.ShapeDtypeStruct((1, LANES), F32)],
        scratch_shapes=[pltpu.VMEM((1, LANES), F32)],
        compiler_params=_params("arbitrary"),
    )(u_b, dq, dk, dv, dgate, dcum, fb, qg, kg)


def _merge(mix_a, mix_b, u_g, x, tgt, wa, wb, wo, fg, tm=256):
    s, d = x.shape

    def body(ma_ref, mb_ref, ug_ref, x_ref, t_ref, wa_ref, wb_ref, wo_ref, fg_ref,
             dx2_ref, dma_ref, dmb_ref, dug_ref, dwa_ref, dwb_ref, dwo_ref, dfg_ref, loss_ref):
        i = pl.program_id(0)

        @pl.when(i == 0)
        def _():
            for ref in (dwa_ref, dwb_ref, dwo_ref, dfg_ref, loss_ref):
                ref[...] = jnp.zeros_like(ref)

        wa_v, wb_v, wo_v, fg_v = wa_ref[...], wb_ref[...], wo_ref[...], fg_ref[...]
        ma = ma_ref[...].astype(BF16)
        mb = mb_ref[...].astype(BF16)
        ya = jnp.dot(ma, wa_v, preferred_element_type=F32)
        yb = jnp.dot(mb, wb_v, preferred_element_type=F32)
        sa = _sigmoid(ug_ref[:, 0:d])
        sb = _sigmoid(ug_ref[:, d:2 * d])
        merged = (sa * ya + sb * yb).astype(BF16)
        x2 = x_ref[...] + jnp.dot(merged, wo_v, preferred_element_type=F32)
        r2 = lax.rsqrt(jnp.mean(x2 * x2, axis=-1, keepdims=True) + RMS_EPS)
        x2h = x2 * r2
        err = x2h * fg_v - t_ref[...]
        loss_ref[...] += _colsum(err * err)
        dy = err * (1.0 / d)
        dfg_ref[...] += _colsum(dy * x2h)
        dx2h = dy * fg_v
        dx2 = r2 * (dx2h - x2h * jnp.mean(dx2h * x2h, axis=-1, keepdims=True))
        dx2_ref[...] = dx2
        dx2b = dx2.astype(BF16)
        dmerged = _dot_nt(dx2b, wo_v)
        dwo_ref[...] += _dot_tn(merged, dx2b)
        dya = dmerged * sa
        dyb = dmerged * sb
        dug_ref[:, 0:d] = dya * ya * (1.0 - sa)
        dug_ref[:, d:2 * d] = dyb * yb * (1.0 - sb)
        dyab = dya.astype(BF16)
        dybb = dyb.astype(BF16)
        dma_ref[...] = _dot_nt(dyab, wa_v)
        dmb_ref[...] = _dot_nt(dybb, wb_v)
        dwa_ref[...] += _dot_tn(ma, dyab)
        dwb_ref[...] += _dot_tn(mb, dybb)

    row = lambda w: pl.BlockSpec((tm, w), lambda i: (i, 0))
    full = lambda a: pl.BlockSpec(a.shape, lambda i: (0, 0))
    fshape = lambda a: jax.ShapeDtypeStruct(a.shape, F32)
    return pl.pallas_call(
        body, name="merge_fwd_bwd", grid=(s // tm,),
        in_specs=[row(D_HALF), row(D_HALF), row(GATE_COLS), row(d), row(d), full(wa), full(wb), full(wo), full(fg)],
        out_specs=[row(d), row(D_HALF), row(D_HALF), row(GATE_COLS), full(wa), full(wb), full(wo), full(fg), full(fg)],
        out_shape=[jax.ShapeDtypeStruct((s, d), F32), jax.ShapeDtypeStruct((s, D_HALF), F32),
                   jax.ShapeDtypeStruct((s, D_HALF), F32), jax.ShapeDtypeStruct((s, GATE_COLS), F32),
                   fshape(wa), fshape(wb), fshape(wo), fshape(fg), fshape(fg)],
        compiler_params=_params("arbitrary"),
    )(mix_a, mix_b, u_g, x, tgt, wa, wb, wo, fg)


def _lora_weight(w_up, a_up):
    z = jnp.zeros((LORA, D_HALF), w_up.dtype)
    return jnp.concatenate([jnp.concatenate([w_up, z], axis=1), jnp.concatenate([z, a_up], axis=1)], axis=0)


def _device_grads(x, tgt, p, w_a, w_up, a_up, late_weights, fwd_exchange=None, bwd_exchange=None, tail_exchange=None):
    wl = _lora_weight(w_up, a_up)
    rk = p["r_k"].reshape(1, D_HALF)
    fb = jnp.pad(p["f_bias"], ((0, 0), (0, LANES - N_HEADS)))
    qg = jnp.tile(p["q_norm_g"], (1, N_HEADS))
    kg = jnp.tile(p["k_norm_g"], (1, N_HEADS))
    fg = p["final_norm_g"].reshape(1, D_MODEL)
    mixer = (p["shift_mu"], wl, p["w0"], p["a0"], p["k_k"], p["k_a"])

    h = _rmsnorm_in(x, p["norm_g"])
    u_a = _matmul_nn(h, w_a, "inproj_rwkv")
    r, dec, k, v, av, bv, gate_a = _rwkv_prep(u_a, *mixer)
    y, st, arrived = _wkv_fwd(r, dec, k, av, bv, v, fwd_exchange)
    mix_a = _rwkv_post(y, r, k, v, gate_a, p["lnx_w"], p["lnx_b"], rk)

    w_b, w_g, w_out_a, w_out_b, w_out = late_weights(arrived)
    u_b = _matmul_nn(h, w_b, "inproj_fox")
    u_g = _matmul_nn(h, w_g, "inproj_gate")
    q, kn, vb, cc, cr = _fox_prep(u_b, fb, qg, kg)
    o, mix_b, lse = _attn_fwd(q, kn, vb, cc, cr, u_b)

    dx2, dmix_a, dmix_b, du_g, dwa, dwb, dwo, dfg, loss_vec = _merge(
        mix_a, mix_b, u_g, x, tgt, w_out_a, w_out_b, w_out, fg)

    do, dgate_b = _fox_post_bwd(dmix_b, o, u_b)
    dd = _attn_bwd_rowdot(q, kn, vb, do, lse, cc, cr)
    dq, dk_att, dv_att, dcr = _attn_bwd(q, kn, vb, do, lse, dd, cc, cr)
    dcum = jnp.pad(dcr.T, ((0, 0), (0, LANES - N_HEADS)))
    du_b, dqg, dkg, dfb = _fox_prep_bwd(u_b, dq, dk_att, dv_att, dgate_b, dcum, fb, qg, kg)
    h_t = h.T
    dw_b = _matmul_tn_acc(h_t, du_b, "dw_fox")
    dw_g = _matmul_tn_acc(h_t, du_g, "dw_gate")

    dy, dr_b, dk_b, dv_b, dgate_a, dlw, dlb, drk = _rwkv_post_bwd(
        dmix_a, y, r, k, v, gate_a, p["lnx_w"], p["lnx_b"], rk)
    scan_grads, sent = _wkv_bwd(r, dec, k, av, bv, v, dy, st,
                                bwd_exchange(dw_b, dw_g, dwa, dwb, dwo) if bwd_exchange else None)
    du_a, dmu, dwl, dw0, da0, dkkw, dkaw = _rwkv_prep_bwd(u_a, (*scan_grads, dr_b, dk_b, dv_b, dgate_a), *mixer)
    dw_a = _matmul_tn_acc(h_t, du_a, "dw_rwkv")
    dw_up, da_up = dwl[:LORA, :D_HALF], dwl[LORA:, D_HALF:]
    grad_x, dnorm_g, sent_last = _inproj_bwd(du_a, du_b, du_g, w_a, w_b, w_g, x, dx2, p["norm_g"],
                                            tail_exchange(dw_a, dw_up, da_up) if tail_exchange else None)

    grads = dict(
        norm_g=dnorm_g, w_in=(dw_a, dw_b, dw_g), shift_mu=dmu,
        w_lora_up=dw_up, w0=dw0, a_lora_up=da_up, a0=da0, k_k=dkkw, k_a=dkaw,
        r_k=drk.reshape(1, N_HEADS, HEAD), lnx_w=dlw, lnx_b=dlb, f_bias=dfb[:, :N_HEADS],
        q_norm_g=dqg.reshape(N_HEADS, HEAD).sum(axis=0, keepdims=True),
        k_norm_g=dkg.reshape(N_HEADS, HEAD).sum(axis=0, keepdims=True),
        w_out_a=dwa, w_out_b=dwb, w_out=dwo, final_norm_g=dfg.reshape(D_MODEL))
    return loss_vec, grad_x, grads, sent, sent_last


CHIP_FLIPS = ((1, 0), (0, 1), (1, 1))
ANY = pl.BlockSpec(memory_space=pl.ANY)


def _position():
    return lax.axis_index("x"), lax.axis_index("y"), lax.axis_index("c")


def _flip(v, f):
    return 1 - v if f else v


def _both(a, b):
    if a is None:
        return b
    return a if b is None else jnp.logical_and(a, b)


def _when(cond, fn):
    if cond is None:
        fn()
    else:
        pl.when(cond)(fn)


class _Moves:
    def __init__(self, send_sems, recv_sems, local_sems):
        self.send_sems, self.recv_sems, self.local_sems = send_sems, recv_sems, local_sems
        self.remote, self.local = [], []

    def send(self, src, dst, peer, landing, send_if=None, recv_if=None):
        k = len(self.remote)
        sems = dict(send_sem=self.send_sems.at[k], recv_sem=self.recv_sems.at[k], device_id=peer, device_id_type=MESH)
        out = pltpu.make_async_remote_copy(src_ref=src, dst_ref=dst, **sems)
        arrival = pltpu.make_async_remote_copy(src_ref=src, dst_ref=landing, **sems)
        self.remote.append((out, arrival, send_if, recv_if))

    def copy(self, src, dst, cond=None):
        cp = pltpu.make_async_copy(src, dst, self.local_sems.at[len(self.local)])
        self.local.append((cp, cond))

    def start(self, also=None):
        for cp, cond in self.local:
            _when(_both(also, cond), cp.start)
        for out, _, send_if, _ in self.remote:
            _when(_both(also, send_if), out.start)

    def wait_arrivals(self, also=None):
        for _, arrival, _, recv_if in self.remote:
            _when(_both(also, recv_if), arrival.wait_recv)

    def wait_sent(self, also=None):
        for out, _, send_if, _ in self.remote:
            _when(_both(also, send_if), out.wait_send)
        for cp, cond in self.local:
            _when(_both(also, cond), cp.wait)

    def wait(self, also=None):
        self.wait_arrivals(also)
        self.wait_sent(also)


class _Exchange:
    def __init__(self, operands, out_shapes, n_remote, n_local, build, n_relay=0, relay=None):
        self.operands, self.out_shapes = list(operands), list(out_shapes)
        self.n_remote, self.n_local, self.build = n_remote, n_local, build
        self.n_relay, self.relay = n_relay, relay

    def scratch(self):
        return [pltpu.SemaphoreType.DMA((self.n_remote,)), pltpu.SemaphoreType.DMA((self.n_remote,)),
                pltpu.SemaphoreType.DMA((max(self.n_local, 1),))]

    def moves(self, in_refs, out_refs, sems):
        mv = _Moves(*sems)
        self.build(mv, in_refs, out_refs)
        return mv

    def run_alone(self, name):
        n_in, n_out = len(self.operands), len(self.out_shapes)
        relay_scratch = [pltpu.SemaphoreType.DMA((self.n_relay,))] * 2 if self.relay else []

        def body(*refs):
            ins, outs, sems = refs[:n_in], refs[n_in:n_in + n_out], refs[n_in + n_out:]
            mv = self.moves(ins, outs, sems[:3])
            mv.start()
            mv.wait_arrivals()
            if self.relay:
                passed = _Moves(sems[3], sems[4], None)
                self.relay(passed, ins, outs)
                passed.start()
                passed.wait()
            mv.wait_sent()

        return pl.pallas_call(
            body, name=name, in_specs=[ANY] * n_in, out_specs=[ANY] * n_out, out_shape=self.out_shapes,
            scratch_shapes=self.scratch() + relay_scratch, compiler_params=pltpu.CompilerParams(has_side_effects=True),
        )(*self.operands)


def _is_chip(x, y, chip):
    return jnp.logical_and(x == chip // 2, y == chip % 2)


def _gather_exchange(from_chip, from_all, split=()):
    n1, n2 = len(from_chip), len(from_all)

    def rows_of(t, c):
        half = from_chip[t][1].shape[0] // 2
        return pl.ds(c * half, half)

    def build(mv, ins, outs):
        x, y, c = _position()
        me = 2 * x + y
        for t, (chip, _) in enumerate(from_chip):
            mv.copy(ins[t], outs[t], cond=_is_chip(x, y, chip))
        for t in range(n2):
            mv.copy(ins[n1 + t], outs[n1 + t].at[me])
        for fx, fy in CHIP_FLIPS:
            px, py = _flip(x, fx), _flip(y, fy)
            peer = (px, py, c)
            for t, (chip, _) in enumerate(from_chip):
                part = rows_of(t, c) if t in split else slice(None)
                mv.send(ins[t].at[part], outs[t].at[part], peer, landing=outs[t].at[part],
                        send_if=_is_chip(x, y, chip), recv_if=_is_chip(px, py, chip))
            for t in range(n2):
                mv.send(ins[n1 + t], outs[n1 + t].at[me], peer, landing=outs[n1 + t].at[2 * px + py])

    def relay(mv, ins, outs):
        x, y, c = _position()
        for t in split:
            came = jnp.logical_not(_is_chip(x, y, from_chip[t][0]))
            mv.send(outs[t].at[rows_of(t, c)], outs[t].at[rows_of(t, c)], (x, y, 1 - c),
                    landing=outs[t].at[rows_of(t, 1 - c)], send_if=came, recv_if=came)

    arrays = [a for _, a in from_chip] + list(from_all)
    shapes = [jax.ShapeDtypeStruct(a.shape, a.dtype) for _, a in from_chip]
    shapes += [jax.ShapeDtypeStruct((N_CHIPS,) + a.shape, a.dtype) for a in from_all]
    return _Exchange(arrays, shapes, len(CHIP_FLIPS) * (n1 + n2), n1 + n2, build,
                     n_relay=len(split), relay=relay if split else None)


def _scatter_exchange(to_chip, to_all):
    n1, n2 = len(to_chip), len(to_all)

    def build(mv, ins, outs):
        x, y, c = _position()
        for f, (fx, fy) in enumerate(CHIP_FLIPS):
            px, py = _flip(x, fx), _flip(y, fy)
            peer = (px, py, c)
            for t, (chip, _) in enumerate(to_chip):
                mv.send(ins[t], outs[t].at[f], peer, landing=outs[t].at[f],
                        send_if=_is_chip(px, py, chip), recv_if=_is_chip(x, y, chip))
            for t in range(n2):
                mv.send(ins[n1 + t].at[2 * px + py], outs[n1 + t].at[f], peer, landing=outs[n1 + t].at[f])

    arrays = [a for _, a in to_chip] + list(to_all)
    shapes = [jax.ShapeDtypeStruct((len(CHIP_FLIPS),) + a.shape, a.dtype) for _, a in to_chip]
    shapes += [jax.ShapeDtypeStruct((len(CHIP_FLIPS),) + a.shape[1:], a.dtype) for a in to_all]
    return _Exchange(arrays, shapes, len(CHIP_FLIPS) * (n1 + n2), 0, build)


def _swap_sibling(tensors):
    n = len(tensors)

    def body(*refs):
        ins, outs = refs[:n], refs[n:2 * n]
        send_sems, recv_sems = refs[2 * n:]
        x, y, c = _position()
        copies = [pltpu.make_async_remote_copy(
            src_ref=ins[t], dst_ref=outs[t], send_sem=send_sems.at[t], recv_sem=recv_sems.at[t],
            device_id=(x, y, 1 - c), device_id_type=MESH) for t in range(n)]
        for cp in copies:
            cp.start()
        for cp in copies:
            cp.wait_recv()
        for cp in copies:
            cp.wait_send()

    return pl.pallas_call(
        body, name="swap_sibling", in_specs=[ANY] * n, out_specs=[ANY] * n,
        out_shape=[jax.ShapeDtypeStruct(a.shape, a.dtype) for a in tensors],
        scratch_shapes=[pltpu.SemaphoreType.DMA((n,)), pltpu.SemaphoreType.DMA((n,))],
        compiler_params=pltpu.CompilerParams(has_side_effects=True),
    )(*tensors)


def _allreduce_small(slab):
    stages = 3

    def body(x_ref, o_ref, buf, send_sems, recv_sems):
        x, y, c = _position()
        peers = ((1 - x, y, c), (x, 1 - y, c), (x, y, 1 - c))
        o_ref[...] = x_ref[...]
        for k, peer in enumerate(peers):
            cp = pltpu.make_async_remote_copy(src_ref=o_ref, dst_ref=buf.at[k], send_sem=send_sems.at[k],
                                              recv_sem=recv_sems.at[k], device_id=peer, device_id_type=MESH)
            cp.start()
            cp.wait()
            o_ref[...] = o_ref[...] + buf[k]

    return pl.pallas_call(
        body, name="allreduce_small",
        in_specs=[pl.BlockSpec(memory_space=pltpu.VMEM)], out_specs=pl.BlockSpec(memory_space=pltpu.VMEM),
        out_shape=jax.ShapeDtypeStruct(slab.shape, slab.dtype),
        scratch_shapes=[pltpu.VMEM((stages,) + slab.shape, slab.dtype),
                        pltpu.SemaphoreType.DMA((stages,)), pltpu.SemaphoreType.DMA((stages,))],
        compiler_params=pltpu.CompilerParams(has_side_effects=True),
    )(slab)


def _row_tile(r):
    return min(r, 256)


def _sum4(stack, recv, me):
    _, r, c = stack.shape
    tr = _row_tile(r)

    def body(me_ref, own_ref, recv_ref, o_ref):
        o_ref[...] = (((own_ref[...] + recv_ref[0].astype(F32)) + recv_ref[1].astype(F32))
                      + recv_ref[2].astype(F32))

    return pl.pallas_call(
        body, name="sum_partials",
        grid_spec=pltpu.PrefetchScalarGridSpec(
            num_scalar_prefetch=1, grid=(r // tr,),
            in_specs=[pl.BlockSpec((None, tr, c), lambda i, me_ref: (me_ref[0], i, 0)),
                      pl.BlockSpec((len(CHIP_FLIPS), tr, c), lambda i, me_ref: (0, i, 0))],
            out_specs=pl.BlockSpec((tr, c), lambda i, me_ref: (i, 0))),
        out_shape=jax.ShapeDtypeStruct((r, c), F32), compiler_params=_params("parallel"),
    )(me, stack, recv)


def _sum_block(own, recv):
    r, c = own.shape
    tr = _row_tile(r)

    def body(own_ref, recv_ref, o_ref):
        o_ref[...] = (((own_ref[...] + recv_ref[0].astype(F32)) + recv_ref[1].astype(F32))
                      + recv_ref[2].astype(F32))

    return pl.pallas_call(
        body, name="sum_block", grid=(r // tr,),
        in_specs=[pl.BlockSpec((tr, c), lambda i: (i, 0)), pl.BlockSpec((len(CHIP_FLIPS), tr, c), lambda i: (0, i, 0))],
        out_specs=pl.BlockSpec((tr, c), lambda i: (i, 0)),
        out_shape=jax.ShapeDtypeStruct((r, c), F32), compiler_params=_params("parallel"),
    )(own, recv)


def _adamw_math(w, g, m, v):
    m = ADAM_B1 * m + (1.0 - ADAM_B1) * g
    v = ADAM_B2 * v + (1.0 - ADAM_B2) * (g * g)
    m_hat = m / (1.0 - ADAM_B1 ** ADAM_STEP)
    v_hat = v / (1.0 - ADAM_B2 ** ADAM_STEP)
    delta = -ADAM_LR * (m_hat / (jnp.sqrt(v_hat) + ADAM_EPS) + ADAM_WD * w)
    return delta, m, v


def _adamw(w, m, v, g_parts, name):
    r, c = w.shape
    tr = _row_tile(r)
    n = len(g_parts)

    def body(*refs):
        w_ref, m_ref, v_ref = refs[:3]
        g_refs = refs[3:3 + n]
        g_out, d_out, m_out, v_out = refs[3 + n:]
        g = g_refs[0][...]
        for ref in g_refs[1:]:
            g = g + ref[...]
        g_out[...] = g
        d_out[...], m_out[...], v_out[...] = _adamw_math(w_ref[...], g, m_ref[...], v_ref[...])

    blk = pl.BlockSpec((tr, c), lambda i: (i, 0))
    return pl.pallas_call(
        body, name=name, grid=(r // tr,), in_specs=[blk] * (3 + n), out_specs=[blk] * 4,
        out_shape=[jax.ShapeDtypeStruct((r, c), F32)] * 4, compiler_params=_params("parallel"),
    )(w, m, v, *g_parts)


SHARDED = ("w_in", "w_lora_up", "a_lora_up", "w_out_a", "w_out_b", "w_out")
ROW_SHARDED = ("w_out",)
SMALL = ("norm_g", "shift_mu", "w0", "a0", "k_k", "k_a", "r_k", "lnx_w", "lnx_b", "f_bias", "q_norm_g", "k_norm_g",
         "final_norm_g")
WEIGHTS = ("norm_g", "w_in", "shift_mu", "w_lora_up", "w0", "a_lora_up", "a0", "k_k", "k_a", "r_k", "lnx_w", "lnx_b",
           "f_bias", "q_norm_g", "k_norm_g", "w_out_a", "w_out_b", "w_out", "final_norm_g")
SLAB_ROWS = 16
SLAB_COLS = SEC


def _to_slab(named, extra=None):
    rows = [jnp.pad(named[n].reshape(1, -1), ((0, 0), (0, SLAB_COLS - named[n].size))) for n in SMALL]
    if extra is not None:
        rows.append(jnp.pad(extra.reshape(1, -1), ((0, 0), (0, SLAB_COLS - extra.size))))
    rows.append(jnp.zeros((SLAB_ROWS - len(rows), SLAB_COLS), F32))
    return jnp.concatenate(rows, axis=0)


def _from_slab(slab, shapes):
    return {n: slab[i, :math.prod(shapes[n])].reshape(shapes[n]) for i, n in enumerate(SMALL)}


def _by_chip(g, name):
    if name in ROW_SHARDED:
        return g.reshape(N_CHIPS, g.shape[0] // N_CHIPS, g.shape[1])
    r, c = g.shape
    return g.reshape(r, N_CHIPS, c // N_CHIPS).transpose(1, 0, 2)


def _from_chips(stack, name):
    if name in ROW_SHARDED:
        return stack.reshape(-1, stack.shape[2])
    _, r, c = stack.shape
    return stack.transpose(1, 0, 2).reshape(r, N_CHIPS * c)


def kernel(x, norm_g, w_in, shift_mu, w_lora_up, w0, a_lora_up, a0, k_k, k_a, r_k, lnx_w, lnx_b, f_bias, q_norm_g, k_norm_g, w_out_a, w_out_b, w_out, final_norm_g, loss_target, m_norm_g, m_w_in, m_shift_mu, m_w_lora_up, m_w0, m_a_lora_up, m_a0, m_k_k, m_k_a, m_r_k, m_lnx_w, m_lnx_b, m_f_bias, m_q_norm_g, m_k_norm_g, m_w_out_a, m_w_out_b, m_w_out, m_final_norm_g, v_norm_g, v_w_in, v_shift_mu, v_w_lora_up, v_w0, v_a_lora_up, v_a0, v_k_k, v_k_a, v_r_k, v_lnx_w, v_lnx_b, v_f_bias, v_q_norm_g, v_k_norm_g, v_w_out_a, v_w_out_b, v_w_out, v_final_norm_g):
    w = dict(norm_g=norm_g, w_in=w_in, shift_mu=shift_mu, w_lora_up=w_lora_up, w0=w0, a_lora_up=a_lora_up, a0=a0,
             k_k=k_k, k_a=k_a, r_k=r_k, lnx_w=lnx_w, lnx_b=lnx_b, f_bias=f_bias, q_norm_g=q_norm_g,
             k_norm_g=k_norm_g, w_out_a=w_out_a, w_out_b=w_out_b, w_out=w_out, final_norm_g=final_norm_g)
    m = dict(norm_g=m_norm_g, w_in=m_w_in, shift_mu=m_shift_mu, w_lora_up=m_w_lora_up, w0=m_w0,
             a_lora_up=m_a_lora_up, a0=m_a0, k_k=m_k_k, k_a=m_k_a, r_k=m_r_k, lnx_w=m_lnx_w, lnx_b=m_lnx_b,
             f_bias=m_f_bias, q_norm_g=m_q_norm_g, k_norm_g=m_k_norm_g, w_out_a=m_w_out_a, w_out_b=m_w_out_b,
             w_out=m_w_out, final_norm_g=m_final_norm_g)
    v = dict(norm_g=v_norm_g, w_in=v_w_in, shift_mu=v_shift_mu, w_lora_up=v_w_lora_up, w0=v_w0,
             a_lora_up=v_a_lora_up, a0=v_a0, k_k=v_k_k, k_a=v_k_a, r_k=v_r_k, lnx_w=v_lnx_w, lnx_b=v_lnx_b,
             f_bias=v_f_bias, q_norm_g=v_q_norm_g, k_norm_g=v_k_norm_g, w_out_a=v_w_out_a, w_out_b=v_w_out_b,
             w_out=v_w_out, final_norm_g=v_final_norm_g)
    shapes = {n: w[n].shape for n in WEIGHTS}

    shard = {n: w[n][0].astype(BF16) for n in SHARDED}
    late = ("w_out_a", "w_out_b", "w_out")
    loras = ("w_lora_up", "a_lora_up")
    w_in_head, w_in_tail = shard["w_in"][:, :A_TAIL], shard["w_in"][:, A_TAIL:]
    shard0, shard1_head, up_stack, aup_stack = _gather_exchange(
        [(0, shard["w_in"]), (1, w_in_head)], [shard[n] for n in loras], split=(0,)).run_alone("gather_early")
    w_a = jnp.concatenate([shard0, shard1_head], axis=1)

    def late_weights(arrived):
        shard1_tail, shard2, shard3 = arrived[:3]
        w_b = jnp.concatenate([shard1_tail, shard2[:, :B_TAIL], jnp.zeros((D_MODEL, SEC - FOX_REAL), BF16)], axis=1)
        w_g = jnp.concatenate([shard2[:, B_TAIL:], shard3], axis=1)
        return (w_b, w_g, *[_from_chips(s, n) for n, s in zip(late, arrived[3:])])

    own = {}

    def bwd_exchange(dw_b, dw_g, dwa, dwb, dwo):
        own["tail1"] = dw_b[:, :B_HEAD]
        own["block2"] = jnp.concatenate([dw_b[:, B_HEAD:FOX_REAL], dw_g[:, :G_HEAD]], axis=1)
        own["block3"] = dw_g[:, G_HEAD:]
        own.update({n: _by_chip(g, n) for n, g in zip(late, (dwa, dwb, dwo))})
        return _scatter_exchange([(1, own["tail1"].astype(BF16)), (2, own["block2"].astype(BF16)),
                                  (3, own["block3"].astype(BF16))], [own[n].astype(BF16) for n in late])

    def tail_exchange(dw_a, dw_up, da_up):
        own["block0"], own["head1"] = dw_a[:, :SHARD_COLS], dw_a[:, SHARD_COLS:]
        own.update({n: _by_chip(g, n) for n, g in zip(loras, (dw_up, da_up))})
        return _scatter_exchange([(0, own["block0"].astype(BF16)), (1, own["head1"].astype(BF16))],
                                 [own[n].astype(BF16) for n in loras])

    small = {n: w[n] for n in SMALL}
    loss_vec, grad_x, grads, sent, sent_last = _device_grads(
        x[0], loss_target[0], small, w_a, _from_chips(up_stack, "w_lora_up"), _from_chips(aup_stack, "a_lora_up"),
        late_weights, _gather_exchange([(1, w_in_tail), (2, shard["w_in"]), (3, shard["w_in"])], [shard[n] for n in late]),
        bwd_exchange, tail_exchange)

    total = _allreduce_small(_to_slab(grads, extra=loss_vec))
    loss = (0.5 / D_MODEL) * jnp.sum(total[len(SMALL)])
    slab_g, slab_d, slab_m, slab_v = _adamw(_to_slab(w), _to_slab(m), _to_slab(v), [total], "adamw_small")
    out_g, out_d, out_m, out_v = (_from_slab(s, shapes) for s in (total, slab_d, slab_m, slab_v))
    del slab_g

    xpos, ypos, _ = _position()
    me = (2 * xpos + ypos).astype(jnp.int32).reshape(1)
    core_sum = {"w_in": lax.switch(me[0], [
        lambda: _sum_block(own["block0"], sent_last[0]),
        lambda: jnp.concatenate([_sum_block(own["head1"], sent_last[1]), _sum_block(own["tail1"], sent[0])], axis=1),
        lambda: _sum_block(own["block2"], sent[1]),
        lambda: _sum_block(own["block3"], sent[2])])}
    core_sum.update({n: _sum4(own[n], r, me) for n, r in zip(late, sent[3:])})
    core_sum.update({n: _sum4(own[n], r, me) for n, r in zip(loras, sent_last[2:])})
    sibling_sums = _swap_sibling([core_sum[n] for n in SHARDED])
    for n, theirs in zip(SHARDED, sibling_sums):
        g, d, m2, v2 = _adamw(w[n][0], m[n][0], v[n][0], [core_sum[n], theirs], "adamw_" + n)
        out_g[n], out_d[n], out_m[n], out_v[n] = (a.reshape(shapes[n]) for a in (g, d, m2, v2))

    return (loss, grad_x.reshape(x.shape), *[out_g[n] for n in WEIGHTS], *[out_d[n] for n in WEIGHTS],
            *[out_m[n] for n in WEIGHTS], *[out_v[n] for n in WEIGHTS])
```

```python
import functools
import math

import jax
import jax.numpy as jnp
from jax import lax
from jax.experimental import pallas as pl
from jax.experimental.pallas import tpu as pltpu

F32 = jnp.float32
BF16 = jnp.bfloat16

D_MODEL = 1024
D_HALF = 512
HEAD = 64
N_HEADS = 8
LORA = 64
RWKV_COLS = 2176
FOX_REAL = 2056
SEC = 2176
GATE_COLS = 2048
IN_COLS = 6280
N_CHIPS = 4
SHARD_COLS = IN_COLS // N_CHIPS
A_TAIL = RWKV_COLS - SHARD_COLS
B_HEAD = SHARD_COLS - A_TAIL
B_TAIL = FOX_REAL - B_HEAD
G_HEAD = SHARD_COLS - B_TAIL
RMS_EPS = 1e-6
LNX_EPS = 64e-5
ATT_SCALE = HEAD ** -0.5
NEG = -1e30

ADAM_LR = 0.001
ADAM_B1 = 0.9
ADAM_B2 = 0.999
ADAM_EPS = 1e-08
ADAM_WD = 0.01
ADAM_STEP = 10

LANES = 128
SUBLANES = 8
VMEM_LIMIT = 56 * 1024 * 1024
MESH = pl.DeviceIdType.MESH


def _params(*sem):
    return pltpu.CompilerParams(dimension_semantics=sem if sem else None, vmem_limit_bytes=VMEM_LIMIT)


def _sigmoid(x):
    return 1.0 / (1.0 + jnp.exp(-x))


def _log_sigmoid(x):
    return jnp.minimum(x, 0.0) - jnp.log(1.0 + jnp.exp(-jnp.abs(x)))


def _head_ones():
    r = lax.broadcasted_iota(jnp.int32, (LANES, LANES), 0) >> 6
    c = lax.broadcasted_iota(jnp.int32, (LANES, LANES), 1) >> 6
    return (r == c).astype(BF16)


def _split3(x):
    hi = x.astype(BF16)
    r1 = x - hi.astype(F32)
    mid = r1.astype(BF16)
    lo = (r1 - mid.astype(F32)).astype(BF16)
    return hi, mid, lo


def _exact_dot(x, ones_bf16, ones_first=False):
    out = None
    for piece in _split3(x):
        if ones_first:
            t = jnp.dot(ones_bf16, piece, preferred_element_type=F32)
        else:
            t = jnp.dot(piece, ones_bf16, preferred_element_type=F32)
        out = t if out is None else out + t
    return out


def _head_sum(x, bd):
    n = x.shape[1] // LANES
    parts = [_exact_dot(x[:, i * LANES:(i + 1) * LANES], bd) for i in range(n)]
    return parts[0] if n == 1 else jnp.concatenate(parts, axis=1)


def _dot_nt(a, b):
    return lax.dot_general(a, b, (((1,), (1,)), ((), ())), preferred_element_type=F32)


def _dot_tn(a, b):
    return lax.dot_general(a, b, (((0,), (0,)), ((), ())), preferred_element_type=F32)


def _colsum(x):
    return jnp.sum(x, axis=0, keepdims=True)


def _rmsnorm_in(x, g, tm=512):
    s, d = x.shape

    def body(x_ref, g_ref, h_ref):
        xv = x_ref[...]
        r = lax.rsqrt(jnp.mean(xv * xv, axis=-1, keepdims=True) + RMS_EPS)
        h_ref[...] = (xv * r * g_ref[...]).astype(BF16)

    return pl.pallas_call(
        body, name="rmsnorm_in", grid=(s // tm,),
        in_specs=[pl.BlockSpec((tm, d), lambda i: (i, 0)), pl.BlockSpec((1, d), lambda i: (0, 0))],
        out_specs=pl.BlockSpec((tm, d), lambda i: (i, 0)),
        out_shape=jax.ShapeDtypeStruct((s, d), BF16), compiler_params=_params("parallel"),
    )(x, g)


def _matmul_nn(a, b, name, tm=512):
    m, k = a.shape
    n = b.shape[1]

    def body(a_ref, b_ref, o_ref):
        o_ref[...] = jnp.dot(a_ref[...], b_ref[...], preferred_element_type=F32)

    return pl.pallas_call(
        body, name=name, grid=(m // tm,),
        in_specs=[pl.BlockSpec((tm, k), lambda i: (i, 0)), pl.BlockSpec((k, n), lambda i: (0, 0))],
        out_specs=pl.BlockSpec((tm, n), lambda i: (i, 0)),
        out_shape=jax.ShapeDtypeStruct((m, n), F32), compiler_params=_params("parallel"),
    )(a, b)


def _matmul_tn_acc(at, b, name, tk=512):
    m, k = at.shape
    n = b.shape[1]

    def body(a_ref, b_ref, o_ref):
        j = pl.program_id(0)

        @pl.when(j == 0)
        def _():
            o_ref[...] = jnp.zeros_like(o_ref)

        o_ref[...] += jnp.dot(a_ref[...], b_ref[...].astype(BF16), preferred_element_type=F32)

    return pl.pallas_call(
        body, name=name, grid=(k // tk,),
        in_specs=[pl.BlockSpec((m, tk), lambda j: (0, j)), pl.BlockSpec((tk, n), lambda j: (j, 0))],
        out_specs=pl.BlockSpec((m, n), lambda j: (0, 0)),
        out_shape=jax.ShapeDtypeStruct((m, n), F32), compiler_params=_params("arbitrary"),
    )(at, b)


def _inproj_bwd(du_a, du_b, du_g, w_a, w_b, w_g, x, dx2, g, exchange=None, tm=256):
    s, d = x.shape
    nb = s // tm

    def body(*refs):
        ((da_ref, db_ref, dg_ref, wa_ref, wb_ref, wg_ref, x_ref, dx2_ref, g_ref), (gx_ref, gg_ref), _,
         moves) = _split_refs(refs, 9, 2, exchange)
        i = pl.program_id(0)
        if moves:
            moves.start(also=(i == 0))

        @pl.when(i == 0)
        def _():
            gg_ref[...] = jnp.zeros_like(gg_ref)

        dh = _dot_nt(da_ref[...].astype(BF16), wa_ref[...])
        dh += _dot_nt(db_ref[...].astype(BF16), wb_ref[...])
        dh += _dot_nt(dg_ref[...].astype(BF16), wg_ref[...])
        xv = x_ref[...]
        r = lax.rsqrt(jnp.mean(xv * xv, axis=-1, keepdims=True) + RMS_EPS)
        xh = xv * r
        gg_ref[...] += _colsum(dh * xh)
        dxh = dh * g_ref[...]
        gx_ref[...] = dx2_ref[...] + r * (dxh - xh * jnp.mean(dxh * xh, axis=-1, keepdims=True))
        if moves:
            moves.wait(also=(i == nb - 1))

    row = lambda w: pl.BlockSpec((tm, w), lambda i: (i, 0))
    full = lambda a: pl.BlockSpec(a.shape, lambda i: (0, 0))
    ex_in = exchange.operands if exchange else []
    ex_out = exchange.out_shapes if exchange else []
    res = pl.pallas_call(
        body, name="inproj_bwd", grid=(nb,),
        in_specs=[row(SEC), row(SEC), row(GATE_COLS), full(w_a), full(w_b), full(w_g), row(d), row(d), full(g)]
                 + [ANY] * len(ex_in),
        out_specs=[row(d), pl.BlockSpec((1, d), lambda i: (0, 0))] + [ANY] * len(ex_out),
        out_shape=[jax.ShapeDtypeStruct((s, d), F32), jax.ShapeDtypeStruct((1, d), F32)] + ex_out,
        scratch_shapes=exchange.scratch() if exchange else [],
        compiler_params=_params("arbitrary"),
    )(du_a, du_b, du_g, w_a, w_b, w_g, x, dx2, g, *ex_in)
    return res[0], res[1], list(res[2:])


def _rwkv_elementwise(ua, prev_row, first, mu, wl, w0, a0, kkw, kaw, bd):
    tm = ua.shape[0]
    rows = lax.broadcasted_iota(jnp.int32, (tm, 1), 0)
    prev = jnp.where(first, jnp.zeros_like(prev_row), prev_row)
    shifted = jnp.where(rows == 0, prev, pltpu.roll(ua, 1, 0))
    delta = shifted - ua
    us = ua + delta * mu
    r = us[:, 0:512]
    k0 = us[:, 512:1024]
    v = us[:, 1024:1536]
    lo = us[:, 1536:1664]
    gate = us[:, 1664:2176]
    lane = lax.broadcasted_iota(jnp.int32, (1, LANES), 1)
    th = jnp.tanh(lo)
    lin = jnp.where(lane < LORA, th, lo)
    ll = jnp.dot(lin.astype(BF16), wl, preferred_element_type=F32)
    sz = _sigmoid(w0 + ll[:, :512])
    e = sz * math.exp(-0.5)
    dec = jnp.exp(-e)
    a = _sigmoid(a0 + ll[:, 512:])
    kk0 = k0 * kkw
    ss = _head_sum(kk0 * kk0, bd)
    nrm = jnp.maximum(jnp.sqrt(ss), 1e-12)
    kk = kk0 / nrm
    k = k0 * (1.0 + (a - 1.0) * kaw)
    return dict(delta=delta, us=us, r=r, k0=k0, v=v, lo=lo, gate=gate, th=th, lin=lin, sz=sz, e=e, dec=dec,
                a=a, kk0=kk0, ss=ss, nrm=nrm, kk=kk, k=k)


def _rwkv_prep(u_a, mu, wl, w0, a0, kkw, kaw, tm=256):
    s = u_a.shape[0]

    def body(ua_ref, prev_ref, mu_ref, wl_ref, w0_ref, a0_ref, kkw_ref, kaw_ref,
             r_ref, w_ref, k_ref, v_ref, a_ref, b_ref, g_ref):
        i = pl.program_id(0)
        f = _rwkv_elementwise(ua_ref[...], prev_ref[7:8, :], i == 0, mu_ref[...], wl_ref[...], w0_ref[...],
                              a0_ref[...], kkw_ref[...], kaw_ref[...], _head_ones())
        r_ref[...] = f["r"]
        w_ref[...] = f["dec"]
        k_ref[...] = f["k"]
        v_ref[...] = f["v"]
        a_ref[...] = -f["kk"]
        b_ref[...] = f["kk"] * f["a"]
        g_ref[...] = f["gate"]

    vec = lambda w: pl.BlockSpec((1, w), lambda i: (0, 0))
    out = pl.BlockSpec((tm, D_HALF), lambda i: (i, 0))
    return pl.pallas_call(
        body, name="rwkv_prep", grid=(s // tm,),
        in_specs=[pl.BlockSpec((tm, SEC), lambda i: (i, 0)),
                  pl.BlockSpec((8, SEC), lambda i: (jnp.maximum(i * (tm // 8) - 1, 0), 0)),
                  vec(SEC), pl.BlockSpec((LANES, 2 * D_HALF), lambda i: (0, 0)),
                  vec(D_HALF), vec(D_HALF), vec(D_HALF), vec(D_HALF)],
        out_specs=[out] * 7,
        out_shape=[jax.ShapeDtypeStruct((s, D_HALF), F32)] * 7,
        compiler_params=_params("parallel"),
    )(u_a, u_a, mu, wl, w0, a0, kkw, kaw)


SCAN_TB = 128
N_PAIRS = 4


def _pair_sum(x, left):
    s_l = jnp.sum(jnp.where(left, x, 0.0), axis=1, keepdims=True)
    s_r = jnp.sum(jnp.where(left, 0.0, x), axis=1, keepdims=True)
    return jnp.where(left, s_l, s_r)


def _quad_consts():
    lane = lax.broadcasted_iota(jnp.int32, (HEAD, 2 * LANES), 1)
    rowi = lax.broadcasted_iota(jnp.int32, (HEAD, 2 * LANES), 0)
    diag2 = rowi == (lane & (HEAD - 1))
    r = lax.broadcasted_iota(jnp.int32, (2 * LANES, 2 * LANES), 0) >> 6
    c = lax.broadcasted_iota(jnp.int32, (2 * LANES, 2 * LANES), 1) >> 6
    return diag2, (r == c).astype(BF16)


def _rows_to_columns(x8, diag2, bd2):
    lhs = jnp.concatenate([jnp.where(diag2, x8[i:i + 1], 0.0).astype(BF16) for i in range(SUBLANES)], axis=0)
    return jnp.dot(lhs, bd2, preferred_element_type=F32)


def _diag_rows(qtile, diag2, bd2, sub_row2):
    res = jnp.dot(qtile, bd2, preferred_element_type=F32)
    out = jnp.zeros((SUBLANES, 2 * LANES), F32)
    for i in range(SUBLANES):
        out = jnp.where(sub_row2 == i, _colsum(jnp.where(diag2, res[i * HEAD:(i + 1) * HEAD], 0.0)), out)
    return out


def _store_tile(qbuf, slot, p, i, x):
    qbuf[slot, p // 2, i * HEAD:(i + 1) * HEAD, (p % 2) * LANES:(p % 2 + 1) * LANES] = x.astype(BF16)


def _left_half():
    return lax.broadcasted_iota(jnp.int32, (HEAD, LANES), 1) < HEAD


def _split_refs(refs, n_rows, n_out, exchange):
    n_in = len(exchange.operands) if exchange else 0
    n_ex_out = len(exchange.out_shapes) if exchange else 0
    refs = list(refs)
    rows, refs = refs[:n_rows], refs[n_rows:]
    ex_in, refs = refs[:n_in], refs[n_in:]
    outs, refs = refs[:n_out], refs[n_out:]
    ex_out, refs = refs[:n_ex_out], refs[n_ex_out:]
    scratch, sems = (refs[:-3], refs[-3:]) if exchange else (refs, None)
    moves = exchange.moves(ex_in, ex_out, sems) if exchange else None
    return rows, outs, scratch, moves


def _wkv_fwd(r, w, k, a, b, v, exchange=None):
    s = r.shape[0]
    tb = SCAN_TB
    nb = s // tb

    def body(*refs):
        (r_ref, w_ref, k_ref, a_ref, b_ref, v_ref), (y_ref, st_ref), (state, vbuf, qbuf), moves = _split_refs(
            refs, 6, 2, exchange)
        g = pl.program_id(0)
        if moves:
            moves.start(also=(g == 0))

        @pl.when(g == 0)
        def _():
            state[...] = jnp.zeros_like(state)
            qbuf[...] = jnp.zeros_like(qbuf)

        left = _left_half()
        diag2, bd2 = _quad_consts()
        sub_row2 = lax.broadcasted_iota(jnp.int32, (SUBLANES, 2 * LANES), 0)
        groups = tb // SUBLANES
        quads = [slice(g2 * 2 * LANES, (g2 + 1) * 2 * LANES) for g2 in range(2)]

        def rows_of(q):
            return pl.ds(pl.multiple_of(q * SUBLANES, SUBLANES), SUBLANES)

        def v_tiles(q, slot):
            v8 = v_ref[rows_of(q), :]
            for g2 in range(2):
                vbuf[slot, g2] = _rows_to_columns(v8[:, quads[g2]], diag2, bd2)

        def chain(q, slot):
            rows8 = rows_of(q)
            a8, w8, b8, k8, r8 = (x[rows8, :] for x in (a_ref, w_ref, b_ref, k_ref, r_ref))
            pairs = [slice(p * LANES, (p + 1) * LANES) for p in range(N_PAIRS)]
            a_next = pltpu.roll(a8, SUBLANES - 1, 0)
            wa8 = w8 * a_next
            ba8 = jnp.concatenate([_pair_sum(b8[:, pr] * a_next[:, pr], left[0:SUBLANES]) for pr in pairs], axis=1)
            ka8 = jnp.concatenate([_pair_sum(k8[:, pr] * a_next[:, pr], left[0:SUBLANES]) for pr in pairs], axis=1)
            sp = [state[p] for p in range(N_PAIRS)]
            for i in range(0, SUBLANES, 2):
                r0, r1 = slice(i, i + 1), slice(i + 1, i + 2)
                sums = [(_pair_sum(sp[p] * a8[r0, pairs[p]], left), _pair_sum(sp[p] * wa8[r0, pairs[p]], left))
                        for p in range(N_PAIRS)]
                sa0, sa1 = [s[0] for s in sums], [s[1] for s in sums]
                for p in range(N_PAIRS):
                    pr = pairs[p]
                    inner = slice((p % 2) * LANES, (p % 2 + 1) * LANES)
                    vt0 = vbuf[slot, p // 2, i * HEAD:(i + 1) * HEAD, inner]
                    vt1 = vbuf[slot, p // 2, (i + 1) * HEAD:(i + 2) * HEAD, inner]
                    sa_next = sa1[p] + sa0[p] * ba8[r0, pr] + vt0 * ka8[r0, pr]
                    s1 = sp[p] * w8[r0, pr] + sa0[p] * b8[r0, pr] + vt0 * k8[r0, pr]
                    st_ref[q * SUBLANES + i, p] = s1
                    _store_tile(qbuf, slot, p, i, s1 * r8[r0, pr])
                    s2 = s1 * w8[r1, pr] + sa_next * b8[r1, pr] + vt1 * k8[r1, pr]
                    st_ref[q * SUBLANES + i + 1, p] = s2
                    _store_tile(qbuf, slot, p, i + 1, s2 * r8[r1, pr])
                    sp[p] = s2
            for p in range(N_PAIRS):
                state[p] = sp[p]

        def y_rows(q, slot):
            for g2 in range(2):
                y_ref[rows_of(q), quads[g2]] = _diag_rows(qbuf[slot, g2], diag2, bd2, sub_row2)

        v_tiles(0, 0)

        def two_groups(j, carry):
            q0 = 2 * j
            v_tiles(q0 + 1, 1)
            chain(q0, 0)
            y_rows(jnp.maximum(q0 - 1, 0), 1)
            v_tiles(jnp.minimum(q0 + 2, groups - 1), 0)
            chain(q0 + 1, 1)
            y_rows(q0, 0)
            return carry

        lax.fori_loop(0, groups // 2, two_groups, 0)
        y_rows(groups - 1, 1)
        if moves:
            moves.wait(also=(g == nb - 1))

    rows = pl.BlockSpec((tb, D_HALF), lambda g: (g, 0))
    ex_in = exchange.operands if exchange else []
    ex_out = exchange.out_shapes if exchange else []
    res = pl.pallas_call(
        body, name="wkv_fwd", grid=(nb,),
        in_specs=[rows] * 6 + [ANY] * len(ex_in),
        out_specs=[rows, pl.BlockSpec((tb, N_PAIRS, HEAD, LANES), lambda g: (g, 0, 0, 0))] + [ANY] * len(ex_out),
        out_shape=[jax.ShapeDtypeStruct((s, D_HALF), F32),
                   jax.ShapeDtypeStruct((s, N_PAIRS, HEAD, LANES), F32)] + ex_out,
        scratch_shapes=[pltpu.VMEM((N_PAIRS, HEAD, LANES), F32),
                        pltpu.VMEM((2, 2, SUBLANES * HEAD, 2 * LANES), F32),
                        pltpu.VMEM((2, 2, SUBLANES * HEAD, 2 * LANES), BF16)]
                       + (exchange.scratch() if exchange else []),
        compiler_params=_params("arbitrary"),
    )(r, w, k, a, b, v, *ex_in)
    return res[0], res[1], list(res[2:])


def _wkv_bwd(r, w, k, a, b, v, dy, st, exchange=None):
    s = r.shape[0]
    tb = SCAN_TB
    nb = s // tb

    def body(*refs):
        ((r_ref, w_ref, k_ref, a_ref, b_ref, v_ref, dy_ref, st_ref, before_ref),
         (dr_ref, dw_ref, dk_ref, dv_ref, da_ref, db_ref), (dstate, vbuf, qbuf), moves) = _split_refs(refs, 9, 6, exchange)
        g = pl.program_id(0)
        first_block = g == nb - 1
        if moves:
            moves.start(also=(g == 0))

        @pl.when(g == 0)
        def _():
            dstate[...] = jnp.zeros_like(dstate)
            qbuf[...] = jnp.zeros_like(qbuf)

        left = _left_half()
        diag2, bd2 = _quad_consts()
        sub_row = lax.broadcasted_iota(jnp.int32, (SUBLANES, LANES), 0)
        sub_row2 = lax.broadcasted_iota(jnp.int32, (SUBLANES, 2 * LANES), 0)
        groups = tb // SUBLANES
        quads = [slice(g2 * 2 * LANES, (g2 + 1) * 2 * LANES) for g2 in range(2)]
        row_refs = (dr_ref, dw_ref, dk_ref, da_ref, db_ref)

        def rows_of(q):
            return pl.ds(pl.multiple_of(q * SUBLANES, SUBLANES), SUBLANES)

        def column_tiles(q, slot):
            rows8 = rows_of(q)
            for kind, ref in enumerate((v_ref, dy_ref)):
                x8 = ref[rows8, :]
                for g2 in range(2):
                    vbuf[slot, kind, g2] = _rows_to_columns(x8[:, quads[g2]], diag2, bd2)

        def chain(q, slot):
            rows8 = rows_of(q)
            a8, w8, b8, k8, r8 = (x[rows8, :] for x in (a_ref, w_ref, b_ref, k_ref, r_ref))
            dsp = [dstate[p] for p in range(N_PAIRS)]
            outs = [[jnp.zeros((SUBLANES, LANES), F32) for _ in row_refs] for _ in range(N_PAIRS)]
            after = [st_ref[q * SUBLANES + SUBLANES - 1, p] for p in range(N_PAIRS)]
            for i in reversed(range(SUBLANES)):
                row = slice(i, i + 1)
                pl_ = [slice(p * LANES, (p + 1) * LANES) for p in range(N_PAIRS)]
                tile = [(p // 2, slice(i * HEAD, (i + 1) * HEAD), slice((p % 2) * LANES, (p % 2 + 1) * LANES))
                        for p in range(N_PAIRS)]
                if i > 0:
                    sp = [st_ref[q * SUBLANES + i - 1, p] for p in range(N_PAIRS)]
                else:
                    sp = [jnp.where(q == 0, jnp.where(first_block, 0.0, before_ref[0, p]),
                                    st_ref[jnp.maximum(q * SUBLANES - 1, 0), p]) for p in range(N_PAIRS)]
                dyt = [vbuf[(slot, 1) + tile[p]] for p in range(N_PAIRS)]
                ds = [dsp[p] + dyt[p] * r8[row, pl_[p]] for p in range(N_PAIRS)]
                dsa = [_pair_sum(ds[p] * b8[row, pl_[p]], left) for p in range(N_PAIRS)]
                sa = [_pair_sum(sp[p] * a8[row, pl_[p]], left) for p in range(N_PAIRS)]
                for p in range(N_PAIRS):
                    ar, wr, br, kr = (x[row, pl_[p]] for x in (a8, w8, b8, k8))
                    vt = vbuf[(slot, 0) + tile[p]]
                    dsp[p] = ds[p] * wr + dsa[p] * ar
                    new = (_colsum(after[p] * dyt[p]), _colsum(ds[p] * sp[p]), _colsum(ds[p] * vt),
                           _colsum(sp[p] * dsa[p]), _colsum(ds[p] * sa[p]))
                    outs[p] = [jnp.where(sub_row == i, n, o) for n, o in zip(new, outs[p])]
                    _store_tile(qbuf, slot, p, i, ds[p] * kr)
                after = sp
            for p in range(N_PAIRS):
                dstate[p] = dsp[p]
                for ref, o in zip(row_refs, outs[p]):
                    ref[rows8, p * LANES:(p + 1) * LANES] = o

        def dv_rows(q, slot):
            for g2 in range(2):
                dv_ref[rows_of(q), quads[g2]] = _diag_rows(qbuf[slot, g2], diag2, bd2, sub_row2)

        column_tiles(groups - 1, 0)

        def two_groups(j, carry):
            q0 = groups - 1 - 2 * j
            column_tiles(q0 - 1, 1)
            chain(q0, 0)
            dv_rows(jnp.minimum(q0 + 1, groups - 1), 1)
            column_tiles(jnp.maximum(q0 - 2, 0), 0)
            chain(q0 - 1, 1)
            dv_rows(q0, 0)
            return carry

        lax.fori_loop(0, groups // 2, two_groups, 0)
        dv_rows(0, 1)
        if moves:
            moves.wait(also=(g == nb - 1))

    rows = pl.BlockSpec((tb, D_HALF), lambda g: (nb - 1 - g, 0))
    ex_in = exchange.operands if exchange else []
    ex_out = exchange.out_shapes if exchange else []
    res = pl.pallas_call(
        body, name="wkv_bwd", grid=(nb,),
        in_specs=[rows] * 7 + [pl.BlockSpec((tb, N_PAIRS, HEAD, LANES), lambda g: (nb - 1 - g, 0, 0, 0)),
                               pl.BlockSpec((1, N_PAIRS, HEAD, LANES),
                                            lambda g: (jnp.maximum((nb - 1 - g) * tb - 1, 0), 0, 0, 0))]
                 + [ANY] * len(ex_in),
        out_specs=[rows] * 6 + [ANY] * len(ex_out),
        out_shape=[jax.ShapeDtypeStruct((s, D_HALF), F32)] * 6 + ex_out,
        scratch_shapes=[pltpu.VMEM((N_PAIRS, HEAD, LANES), F32),
                        pltpu.VMEM((2, 2, 2, SUBLANES * HEAD, 2 * LANES), F32),
                        pltpu.VMEM((2, 2, SUBLANES * HEAD, 2 * LANES), BF16)]
                       + (exchange.scratch() if exchange else []),
        compiler_params=_params("arbitrary"),
    )(r, w, k, a, b, v, dy, st, st, *ex_in)
    return list(res[:6]), list(res[6:])


def _rwkv_post_math(y, r, k, v, gate, lw, lb, rk, bd):
    mean = _head_sum(y, bd) * (1.0 / HEAD)
    yc = y - mean
    var = _head_sum(yc * yc, bd) * (1.0 / HEAD)
    rstd = lax.rsqrt(var + LNX_EPS)
    yn = yc * rstd
    rkk = _head_sum(r * k * rk, bd)
    sg = _sigmoid(gate)
    pre = yn * lw + lb + rkk * v
    return yn, rstd, rkk, sg, pre


def _rwkv_post(y, r, k, v, gate, lw, lb, rk, tm=256):
    s = y.shape[0]

    def body(y_ref, r_ref, k_ref, v_ref, g_ref, lw_ref, lb_ref, rk_ref, o_ref):
        gate_v = g_ref[...]
        _, _, _, sg, pre = _rwkv_post_math(y_ref[...], r_ref[...], k_ref[...], v_ref[...], gate_v,
                                           lw_ref[...], lb_ref[...], rk_ref[...], _head_ones())
        o_ref[...] = pre * (gate_v * sg)

    blk = pl.BlockSpec((tm, D_HALF), lambda i: (i, 0))
    vec = pl.BlockSpec((1, D_HALF), lambda i: (0, 0))
    return pl.pallas_call(
        body, name="rwkv_post", grid=(s // tm,),
        in_specs=[blk] * 5 + [vec] * 3, out_specs=blk,
        out_shape=jax.ShapeDtypeStruct((s, D_HALF), F32), compiler_params=_params("parallel"),
    )(y, r, k, v, gate, lw, lb, rk)


def _rwkv_post_bwd(dmix, y, r, k, v, gate, lw, lb, rk, tm=256):
    s = y.shape[0]

    def body(dm_ref, y_ref, r_ref, k_ref, v_ref, g_ref, lw_ref, lb_ref, rk_ref,
             dy_ref, dr_ref, dk_ref, dv_ref, dg_ref, dlw_ref, dlb_ref, drk_ref):
        i = pl.program_id(0)

        @pl.when(i == 0)
        def _():
            dlw_ref[...] = jnp.zeros_like(dlw_ref)
            dlb_ref[...] = jnp.zeros_like(dlb_ref)
            drk_ref[...] = jnp.zeros_like(drk_ref)

        bd = _head_ones()
        rv, kv, vv, gate_v, lw_v, rk_v = r_ref[...], k_ref[...], v_ref[...], g_ref[...], lw_ref[...], rk_ref[...]
        yn, rstd, rkk, sg, pre = _rwkv_post_math(y_ref[...], rv, kv, vv, gate_v, lw_v, lb_ref[...], rk_v, bd)
        dm = dm_ref[...]
        dg_ref[...] = dm * pre * (sg * (1.0 + gate_v * (1.0 - sg)))
        dpre = dm * (gate_v * sg)
        dlw_ref[...] += _colsum(dpre * yn)
        dlb_ref[...] += _colsum(dpre)
        dyn = dpre * lw_v
        m1 = _head_sum(dyn, bd) * (1.0 / HEAD)
        m2 = _head_sum(dyn * yn, bd) * (1.0 / HEAD)
        dy_ref[...] = rstd * (dyn - m1 - yn * m2)
        dv_ref[...] = dpre * rkk
        drkk = _head_sum(dpre * vv, bd)
        dr_ref[...] = drkk * kv * rk_v
        dk_ref[...] = drkk * rv * rk_v
        drk_ref[...] += _colsum(drkk * rv * kv)

    blk = pl.BlockSpec((tm, D_HALF), lambda i: (i, 0))
    vec = pl.BlockSpec((1, D_HALF), lambda i: (0, 0))
    return pl.pallas_call(
        body, name="rwkv_post_bwd", grid=(s // tm,),
        in_specs=[blk] * 6 + [vec] * 3, out_specs=[blk] * 5 + [vec] * 3,
        out_shape=[jax.ShapeDtypeStruct((s, D_HALF), F32)] * 5 + [jax.ShapeDtypeStruct((1, D_HALF), F32)] * 3,
        compiler_params=_params("arbitrary"),
    )(dmix, y, r, k, v, gate, lw, lb, rk)


def _rwkv_prep_bwd(u_a, grads, mu, wl, w0, a0, kkw, kaw, tm=256):
    s = u_a.shape[0]
    nb = s // tm

    def body(ua_ref, prev_ref, drs_ref, dws_ref, dks_ref, dvs_ref, das_ref, dbs_ref, drb_ref, dkb_ref, dvb_ref,
             dgt_ref, mu_ref, wl_ref, w0_ref, a0_ref, kkw_ref, kaw_ref,
             du_ref, dmu_ref, dwl_ref, dw0_ref, da0_ref, dkkw_ref, dkaw_ref, carry):
        i = pl.program_id(0)

        @pl.when(i == 0)
        def _():
            carry[...] = jnp.zeros_like(carry)
            for ref in (dmu_ref, dwl_ref, dw0_ref, da0_ref, dkkw_ref, dkaw_ref):
                ref[...] = jnp.zeros_like(ref)

        bd = _head_ones()
        mu_v, wl_v, kkw_v, kaw_v = mu_ref[...], wl_ref[...], kkw_ref[...], kaw_ref[...]
        f = _rwkv_elementwise(ua_ref[...], prev_ref[7:8, :], i == nb - 1, mu_v, wl_v, w0_ref[...],
                              a0_ref[...], kkw_v, kaw_v, bd)
        a, kk, k0 = f["a"], f["kk"], f["k0"]
        dk = dks_ref[...] + dkb_ref[...]
        dbs = dbs_ref[...]
        dkk = dbs * a - das_ref[...]
        da = dbs * kk + dk * k0 * kaw_v
        dk0 = dk * (1.0 + (a - 1.0) * kaw_v)
        dkaw_ref[...] += _colsum(dk * k0 * (a - 1.0))
        inv = 1.0 / f["nrm"]
        proj = _head_sum(dkk * kk, bd)
        dkk0 = jnp.where(f["ss"] > 1e-24, (dkk - kk * proj) * inv, dkk * inv)
        dk0 = dk0 + dkk0 * kkw_v
        dkkw_ref[...] += _colsum(dkk0 * k0)
        dza = da * a * (1.0 - a)
        da0_ref[...] += _colsum(dza)
        dz = -dws_ref[...] * f["dec"] * f["e"] * (1.0 - f["sz"])
        dw0_ref[...] += _colsum(dz)
        dll = jnp.concatenate([dz, dza], axis=1).astype(BF16)
        dwl_ref[...] += _dot_tn(f["lin"].astype(BF16), dll)
        dlin = _dot_nt(dll, wl_v)
        lane = lax.broadcasted_iota(jnp.int32, (1, LANES), 1)
        th = f["th"]
        dlo = jnp.where(lane < LORA, dlin * (1.0 - th * th), dlin)
        dus = jnp.concatenate([drs_ref[...] + drb_ref[...], dk0, dvs_ref[...] + dvb_ref[...], dlo, dgt_ref[...]],
                              axis=1)
        dmu_ref[...] += _colsum(dus * f["delta"])
        g1 = dus * mu_v
        rows = lax.broadcasted_iota(jnp.int32, (tm, 1), 0)
        up = jnp.where(rows == tm - 1, carry[...], pltpu.roll(g1, tm - 1, 0))
        du_ref[...] = dus - g1 + up
        carry[...] = g1[0:1, :]

    rev = lambda w: pl.BlockSpec((tm, w), lambda i: (nb - 1 - i, 0))
    vec = lambda w: pl.BlockSpec((1, w), lambda i: (0, 0))
    wl_spec = pl.BlockSpec((LANES, 2 * D_HALF), lambda i: (0, 0))
    return pl.pallas_call(
        body, name="rwkv_prep_bwd", grid=(nb,),
        in_specs=[rev(SEC), pl.BlockSpec((8, SEC), lambda i: (jnp.maximum((nb - 1 - i) * (tm // 8) - 1, 0), 0))]
                 + [rev(D_HALF)] * 10 + [vec(SEC), wl_spec] + [vec(D_HALF)] * 4,
        out_specs=[rev(SEC), vec(SEC), wl_spec] + [vec(D_HALF)] * 4,
        out_shape=[jax.ShapeDtypeStruct((s, SEC), F32), jax.ShapeDtypeStruct((1, SEC), F32),
                   jax.ShapeDtypeStruct((LANES, 2 * D_HALF), F32)] + [jax.ShapeDtypeStruct((1, D_HALF), F32)] * 4,
        scratch_shapes=[pltpu.VMEM((1, SEC), F32)],
        compiler_params=_params("arbitrary"),
    )(u_a, u_a, *grads, mu, wl, w0, a0, kkw, kaw)


def _tri(tm, lower):
    r = lax.broadcasted_iota(jnp.int32, (tm, tm), 0)
    c = lax.broadcasted_iota(jnp.int32, (tm, tm), 1)
    return ((r >= c) if lower else (r <= c)).astype(BF16)


def _head_rms(x, g, bd):
    rinv = lax.rsqrt(_head_sum(x * x, bd) * (1.0 / HEAD) + RMS_EPS)
    xh = x * rinv
    return xh, rinv, xh * g


def _fox_prep(u_b, fb, qg, kg, tm=256):
    s = u_b.shape[0]

    def body(ub_ref, fb_ref, qg_ref, kg_ref, q_ref, k_ref, v_ref, cc_ref, cr_ref, carry):
        i = pl.program_id(0)

        @pl.when(i == 0)
        def _():
            carry[...] = jnp.zeros_like(carry)

        bd = _head_ones()
        _, _, qn = _head_rms(ub_ref[:, 0:512], qg_ref[...], bd)
        _, _, kn = _head_rms(ub_ref[:, 512:1024], kg_ref[...], bd)
        q_ref[...] = (qn * ATT_SCALE).astype(BF16)
        k_ref[...] = kn.astype(BF16)
        v_ref[...] = ub_ref[:, 1024:1536].astype(BF16)
        lane = lax.broadcasted_iota(jnp.int32, (1, LANES), 1)
        logf = jnp.where(lane < N_HEADS, _log_sigmoid(ub_ref[:, 2048:2176] + fb_ref[...]), 0.0)
        cum = _exact_dot(logf, _tri(tm, True), ones_first=True) + carry[...]
        for h in range(N_HEADS):
            cc_ref[h] = jnp.broadcast_to(cum[:, h:h + 1], (tm, LANES))
        cr_ref[...] = jnp.transpose(cum)[0:N_HEADS, :]
        carry[...] = cum[tm - 1:tm, :]

    blk = pl.BlockSpec((tm, D_HALF), lambda i: (i, 0))
    return pl.pallas_call(
        body, name="fox_prep", grid=(s // tm,),
        in_specs=[pl.BlockSpec((tm, SEC), lambda i: (i, 0)), pl.BlockSpec((1, LANES), lambda i: (0, 0)),
                  pl.BlockSpec((1, D_HALF), lambda i: (0, 0)), pl.BlockSpec((1, D_HALF), lambda i: (0, 0))],
        out_specs=[blk, blk, blk, pl.BlockSpec((N_HEADS, tm, LANES), lambda i: (0, i, 0)),
                   pl.BlockSpec((N_HEADS, tm), lambda i: (0, i))],
        out_shape=[jax.ShapeDtypeStruct((s, D_HALF), BF16)] * 3
                  + [jax.ShapeDtypeStruct((N_HEADS, s, LANES), F32), jax.ShapeDtypeStruct((N_HEADS, s), F32)],
        scratch_shapes=[pltpu.VMEM((1, LANES), F32)],
        compiler_params=_params("arbitrary"),
    )(u_b, fb, qg, kg)


ATT_T = 256


def _attn_fwd(q, k, v, cc, cr, u_b):
    s = q.shape[0]
    t = ATT_T
    nblk = s // t

    def body(q_ref, k_ref, v_ref, cc_ref, cr_ref, g_ref, o_ref, mix_ref, lse_ref, m_sc, l_sc, acc_sc):
        i = pl.program_id(0)
        j = pl.program_id(1)

        @pl.when(j == 0)
        def _():
            m_sc[...] = jnp.full_like(m_sc, NEG)
            l_sc[...] = jnp.zeros_like(l_sc)
            acc_sc[...] = jnp.zeros_like(acc_sc)

        def tile(on_diagonal):
            causal = _causal_tile(t) if on_diagonal else None
            left = lax.broadcasted_iota(jnp.int32, (1, LANES), 1) < HEAD
            for p in range(N_PAIRS):
                lanes = slice(p * LANES, (p + 1) * LANES)
                q2, k2, v2 = q_ref[:, lanes], k_ref[:, lanes], v_ref[:, lanes]
                acc2 = acc_sc[:, lanes]
                for e in range(2):
                    h = 2 * p + e
                    msk = left if e == 0 else jnp.logical_not(left)
                    sc = _dot_nt(jnp.where(msk, q2, jnp.zeros_like(q2)), k2)
                    sc = sc + (_wide(cc_ref[h]) - cr_ref[h:h + 1, :])
                    if on_diagonal:
                        sc = jnp.where(causal, sc, NEG)
                    m_prev = m_sc[h]
                    m_new = jnp.maximum(m_prev, jnp.max(sc, axis=1, keepdims=True))
                    alpha = jnp.exp(m_prev - m_new)
                    pm = jnp.exp(sc - _wide(m_new))
                    l_sc[h] = alpha * l_sc[h] + jnp.sum(pm, axis=1, keepdims=True)
                    m_sc[h] = m_new
                    pv = jnp.dot(pm.astype(BF16), v2, preferred_element_type=F32)
                    acc2 = jnp.where(msk, alpha * acc2 + pv, acc2)
                acc_sc[:, lanes] = acc2

        pl.when(j < i)(functools.partial(tile, False))
        pl.when(j == i)(functools.partial(tile, True))

        @pl.when(j == i)
        def _():
            left = lax.broadcasted_iota(jnp.int32, (1, LANES), 1) < HEAD
            for p in range(N_PAIRS):
                lanes = slice(p * LANES, (p + 1) * LANES)
                inv = jnp.where(left, 1.0 / l_sc[2 * p], 1.0 / l_sc[2 * p + 1])
                o = acc_sc[:, lanes] * inv
                o_ref[:, lanes] = o
                gate = g_ref[:, lanes]
                mix_ref[:, lanes] = o * (gate * _sigmoid(gate))
            for h in range(N_HEADS):
                lse_ref[h] = m_sc[h] + jnp.log(l_sc[h])

    qblk = pl.BlockSpec((t, D_HALF), lambda i, j: (i, 0))
    kblk = pl.BlockSpec((t, D_HALF), lambda i, j: (jnp.minimum(i, j), 0))
    return pl.pallas_call(
        body, name="fox_attn_fwd", grid=(nblk, nblk),
        in_specs=[qblk, kblk, kblk, pl.BlockSpec((N_HEADS, t, LANES), lambda i, j: (0, i, 0)),
                  pl.BlockSpec((N_HEADS, t), lambda i, j: (0, jnp.minimum(i, j))),
                  pl.BlockSpec((t, D_HALF), lambda i, j: (i, 3))],
        out_specs=[qblk, qblk, pl.BlockSpec((N_HEADS, t, LANES), lambda i, j: (0, i, 0))],
        out_shape=[jax.ShapeDtypeStruct((s, D_HALF), F32), jax.ShapeDtypeStruct((s, D_HALF), F32),
                   jax.ShapeDtypeStruct((N_HEADS, s, LANES), F32)],
        scratch_shapes=[pltpu.VMEM((N_HEADS, t, LANES), F32), pltpu.VMEM((N_HEADS, t, LANES), F32),
                        pltpu.VMEM((t, D_HALF), F32)],
        compiler_params=_params("parallel", "arbitrary"),
    )(q, k, v, cc, cr, u_b)


def _fox_post_bwd(dmix, o, u_b, tm=256):
    s = o.shape[0]

    def body(dm_ref, o_ref, g_ref, do_ref, dg_ref):
        gate = g_ref[...]
        sg = _sigmoid(gate)
        dm = dm_ref[...]
        do_ref[...] = (dm * (gate * sg)).astype(BF16)
        dg_ref[...] = dm * o_ref[...] * (sg * (1.0 + gate * (1.0 - sg)))

    blk = pl.BlockSpec((tm, D_HALF), lambda i: (i, 0))
    return pl.pallas_call(
        body, name="fox_post_bwd", grid=(s // tm,),
        in_specs=[blk, blk, pl.BlockSpec((tm, D_HALF), lambda i: (i, 3))], out_specs=[blk] * 2,
        out_shape=[jax.ShapeDtypeStruct((s, D_HALF), BF16), jax.ShapeDtypeStruct((s, D_HALF), F32)],
        compiler_params=_params("parallel"),
    )(dmix, o, u_b)


def _causal_tile(t):
    return lax.broadcasted_iota(jnp.int32, (t, t), 0) >= lax.broadcasted_iota(jnp.int32, (t, t), 1)


def _wide(x):
    return jnp.concatenate([x, x], axis=1)


def _attn_probs(q2, k2, v2, do2, msk, causal, bias, lse_rows):
    zero = jnp.zeros_like(q2)
    qh = jnp.where(msk, q2, zero)
    doh = jnp.where(msk, do2, zero)
    sc = _dot_nt(qh, k2) + bias
    if causal is not None:
        sc = jnp.where(causal, sc, NEG)
    pm = jnp.exp(sc - _wide(lse_rows))
    dp = _dot_nt(doh, v2)
    return qh, doh, pm, dp


def _attn_bwd_rowdot(q, k, v, do, lse, cc, cr):
    s = q.shape[0]
    t = ATT_T
    nblk = s // t

    def body(q_ref, k_ref, v_ref, do_ref, lse_ref, cc_ref, cr_ref, dd_ref, acc):
        i = pl.program_id(0)
        j = pl.program_id(1)

        @pl.when(j == 0)
        def _():
            acc[...] = jnp.zeros_like(acc)

        def tile(on_diagonal):
            causal = _causal_tile(t) if on_diagonal else None
            left = lax.broadcasted_iota(jnp.int32, (1, LANES), 1) < HEAD
            for p in range(N_PAIRS):
                lanes = slice(p * LANES, (p + 1) * LANES)
                q2, k2, v2, do2 = q_ref[:, lanes], k_ref[:, lanes], v_ref[:, lanes], do_ref[:, lanes]
                for e in range(2):
                    h = 2 * p + e
                    msk = left if e == 0 else jnp.logical_not(left)
                    bias = _wide(cc_ref[h]) - cr_ref[h:h + 1, :]
                    _, _, pm, dp = _attn_probs(q2, k2, v2, do2, msk, causal, bias, lse_ref[h])
                    acc[h] += jnp.sum(pm * dp, axis=1, keepdims=True)

        pl.when(j < i)(functools.partial(tile, False))
        pl.when(j == i)(functools.partial(tile, True))

        @pl.when(j == i)
        def _():
            dd_ref[...] = acc[...]

    qblk = pl.BlockSpec((t, D_HALF), lambda i, j: (i, 0))
    qcol = pl.BlockSpec((N_HEADS, t, LANES), lambda i, j: (0, i, 0))
    kblk = pl.BlockSpec((t, D_HALF), lambda i, j: (jnp.minimum(i, j), 0))
    return pl.pallas_call(
        body, name="fox_attn_rowdot", grid=(nblk, nblk),
        in_specs=[qblk, kblk, kblk, qblk, qcol, qcol, pl.BlockSpec((N_HEADS, t), lambda i, j: (0, jnp.minimum(i, j)))],
        out_specs=qcol, out_shape=jax.ShapeDtypeStruct((N_HEADS, s, LANES), F32),
        scratch_shapes=[pltpu.VMEM((N_HEADS, t, LANES), F32)],
        compiler_params=_params("parallel", "arbitrary"),
    )(q, k, v, do, lse, cc, cr)


def _attn_bwd(q, k, v, do, lse, dd, cc, cr):
    s = q.shape[0]
    t = ATT_T
    nblk = s // t

    def body(q_ref, k_ref, v_ref, do_ref, lse_ref, dd_ref, cc_ref, cr_ref,
             dq_ref, dk_ref, dv_ref, dcr_ref, dk_sc, dv_sc, dcr_sc):
        j = pl.program_id(0)
        i = pl.program_id(1)

        @pl.when(jnp.logical_and(j == 0, i == 0))
        def _():
            dq_ref[...] = jnp.zeros_like(dq_ref)

        @pl.when(i == 0)
        def _():
            dk_sc[...] = jnp.zeros_like(dk_sc)
            dv_sc[...] = jnp.zeros_like(dv_sc)
            dcr_sc[...] = jnp.zeros_like(dcr_sc)

        def tile(on_diagonal):
            causal = _causal_tile(t) if on_diagonal else None
            left = lax.broadcasted_iota(jnp.int32, (1, LANES), 1) < HEAD
            qrows = pl.ds(pl.multiple_of(i * t, t), t)
            for p in range(N_PAIRS):
                lanes = slice(p * LANES, (p + 1) * LANES)
                q2, k2, v2, do2 = q_ref[:, lanes], k_ref[:, lanes], v_ref[:, lanes], do_ref[:, lanes]
                zero = jnp.zeros_like(q2)
                dq2 = jnp.zeros((t, LANES), F32)
                dk2 = jnp.zeros((t, LANES), F32)
                dv2 = jnp.zeros((t, LANES), F32)
                for e in range(2):
                    h = 2 * p + e
                    msk = left if e == 0 else jnp.logical_not(left)
                    bias = _wide(cc_ref[h]) - cr_ref[h:h + 1, :]
                    qh, doh, pm, dp = _attn_probs(q2, k2, v2, do2, msk, causal, bias, lse_ref[h])
                    dsc = pm * (dp - _wide(dd_ref[h]))
                    dsb = dsc.astype(BF16)
                    dv2 += _dot_tn(pm.astype(BF16), doh)
                    dk2 += _dot_tn(dsb, qh)
                    dq2 += jnp.dot(dsb, jnp.where(msk, k2, zero), preferred_element_type=F32)
                    dcr_sc[h:h + 1, :] += -_colsum(dsc)
                dq_ref[qrows, lanes] += dq2 * ATT_SCALE
                dk_sc[:, lanes] += dk2
                dv_sc[:, lanes] += dv2

        pl.when(i > j)(functools.partial(tile, False))
        pl.when(i == j)(functools.partial(tile, True))

        @pl.when(i == nblk - 1)
        def _():
            dk_ref[...] = dk_sc[...]
            dv_ref[...] = dv_sc[...]
            dcr_ref[...] = dcr_sc[...]

    qblk = pl.BlockSpec((t, D_HALF), lambda j, i: (jnp.maximum(i, j), 0))
    qcol = pl.BlockSpec((N_HEADS, t, LANES), lambda j, i: (0, jnp.maximum(i, j), 0))
    kblk = pl.BlockSpec((t, D_HALF), lambda j, i: (j, 0))
    return pl.pallas_call(
        body, name="fox_attn_bwd", grid=(nblk, nblk),
        in_specs=[qblk, kblk, kblk, qblk, qcol, qcol, qcol, pl.BlockSpec((N_HEADS, t), lambda j, i: (0, j))],
        out_specs=[pl.BlockSpec((s, D_HALF), lambda j, i: (0, 0)), kblk, kblk,
                   pl.BlockSpec((N_HEADS, t), lambda j, i: (0, j))],
        out_shape=[jax.ShapeDtypeStruct((s, D_HALF), F32)] * 3 + [jax.ShapeDtypeStruct((N_HEADS, s), F32)],
        scratch_shapes=[pltpu.VMEM((t, D_HALF), F32), pltpu.VMEM((t, D_HALF), F32), pltpu.VMEM((N_HEADS, t), F32)],
        compiler_params=_params("arbitrary", "arbitrary"),
    )(q, k, v, do, lse, dd, cc, cr)


def _fox_prep_bwd(u_b, dq, dk, dv, dgate, dcum, fb, qg, kg, tm=256):
    s = u_b.shape[0]
    nb = s // tm

    def body(ub_ref, dq_ref, dk_ref, dv_ref, dg_ref, dc_ref, fb_ref, qg_ref, kg_ref,
             du_ref, dqg_ref, dkg_ref, dfb_ref, carry):
        i = pl.program_id(0)

        @pl.when(i == 0)
        def _():
            carry[...] = jnp.zeros_like(carry)
            dqg_ref[...] = jnp.zeros_like(dqg_ref)
            dkg_ref[...] = jnp.zeros_like(dkg_ref)
            dfb_ref[...] = jnp.zeros_like(dfb_ref)

        bd = _head_ones()
        for lo, g_ref, d_ref, dgain_ref in ((0, qg_ref, dq_ref, dqg_ref), (512, kg_ref, dk_ref, dkg_ref)):
            gain = g_ref[...]
            xh, rinv, _ = _head_rms(ub_ref[:, lo:lo + 512], gain, bd)
            dn = d_ref[...]
            dgain_ref[...] += _colsum(dn * xh)
            dxh = dn * gain
            du_ref[:, lo:lo + 512] = rinv * (dxh - xh * (_head_sum(dxh * xh, bd) * (1.0 / HEAD)))
        du_ref[:, 1024:1536] = dv_ref[...]
        du_ref[:, 1536:2048] = dg_ref[...]
        lane = lax.broadcasted_iota(jnp.int32, (1, LANES), 1)
        dc = dc_ref[...]
        dlogf = _exact_dot(dc, _tri(tm, False), ones_first=True) + carry[...]
        carry[...] += _colsum(dc)
        fl = ub_ref[:, 2048:2176] + fb_ref[...]
        dfl = jnp.where(lane < N_HEADS, dlogf * (1.0 - _sigmoid(fl)), 0.0)
        du_ref[:, 2048:2176] = dfl
        dfb_ref[...] += _colsum(dfl)

    rev = lambda w: pl.BlockSpec((tm, w), lambda i: (nb - 1 - i, 0))
    vec = lambda w: pl.BlockSpec((1, w), lambda i: (0, 0))
    return pl.pallas_call(
        body, name="fox_prep_bwd", grid=(nb,),
        in_specs=[rev(SEC)] + [rev(D_HALF)] * 4 + [rev(LANES), vec(LANES), vec(D_HALF), vec(D_HALF)],
        out_specs=[rev(SEC), vec(D_HALF), vec(D_HALF), vec(LANES)],
        out_shape=[jax.ShapeDtypeStruct((s, SEC), F32), jax.ShapeDtypeStruct((1, D_HALF), F32),
                   jax.ShapeDtypeStruct((1, D_HALF), F32), jax.ShapeDtypeStruct((1, LANES), F32)],
        scratch_shapes=[pltpu.VMEM((1, LANES), F32)],
        compiler_params=_params("arbitrary"),
    )(u_b, dq, dk, dv, dgate, dcum, fb, qg, kg)


def _merge(mix_a, mix_b, u_g, x, tgt, wa, wb, wo, fg, tm=256):
    s, d = x.shape

    def body(ma_ref, mb_ref, ug_ref, x_ref, t_ref, wa_ref, wb_ref, wo_ref, fg_ref,
             dx2_ref, dma_ref, dmb_ref, dug_ref, dwa_ref, dwb_ref, dwo_ref, dfg_ref, loss_ref):
        i = pl.program_id(0)

        @pl.when(i == 0)
        def _():
            for ref in (dwa_ref, dwb_ref, dwo_ref, dfg_ref, loss_ref):
                ref[...] = jnp.zeros_like(ref)

        wa_v, wb_v, wo_v, fg_v = wa_ref[...], wb_ref[...], wo_ref[...], fg_ref[...]
        ma = ma_ref[...].astype(BF16)
        mb = mb_ref[...].astype(BF16)
        ya = jnp.dot(ma, wa_v, preferred_element_type=F32)
        yb = jnp.dot(mb, wb_v, preferred_element_type=F32)
        sa = _sigmoid(ug_ref[:, 0:d])
        sb = _sigmoid(ug_ref[:, d:2 * d])
        merged = (sa * ya + sb * yb).astype(BF16)
        x2 = x_ref[...] + jnp.dot(merged, wo_v, preferred_element_type=F32)
        r2 = lax.rsqrt(jnp.mean(x2 * x2, axis=-1, keepdims=True) + RMS_EPS)
        x2h = x2 * r2
        err = x2h * fg_v - t_ref[...]
        loss_ref[...] += _colsum(err * err)
        dy = err * (1.0 / d)
        dfg_ref[...] += _colsum(dy * x2h)
        dx2h = dy * fg_v
        dx2 = r2 * (dx2h - x2h * jnp.mean(dx2h * x2h, axis=-1, keepdims=True))
        dx2_ref[...] = dx2
        dx2b = dx2.astype(BF16)
        dmerged = _dot_nt(dx2b, wo_v)
        dwo_ref[...] += _dot_tn(merged, dx2b)
        dya = dmerged * sa
        dyb = dmerged * sb
        dug_ref[:, 0:d] = dya * ya * (1.0 - sa)
        dug_ref[:, d:2 * d] = dyb * yb * (1.0 - sb)
        dyab = dya.astype(BF16)
        dybb = dyb.astype(BF16)
        dma_ref[...] = _dot_nt(dyab, wa_v)
        dmb_ref[...] = _dot_nt(dybb, wb_v)
        dwa_ref[...] += _dot_tn(ma, dyab)
        dwb_ref[...] += _dot_tn(mb, dybb)

    row = lambda w: pl.BlockSpec((tm, w), lambda i: (i, 0))
    full = lambda a: pl.BlockSpec(a.shape, lambda i: (0, 0))
    fshape = lambda a: jax.ShapeDtypeStruct(a.shape, F32)
    return pl.pallas_call(
        body, name="merge_fwd_bwd", grid=(s // tm,),
        in_specs=[row(D_HALF), row(D_HALF), row(GATE_COLS), row(d), row(d), full(wa), full(wb), full(wo), full(fg)],
        out_specs=[row(d), row(D_HALF), row(D_HALF), row(GATE_COLS), full(wa), full(wb), full(wo), full(fg), full(fg)],
        out_shape=[jax.ShapeDtypeStruct((s, d), F32), jax.ShapeDtypeStruct((s, D_HALF), F32),
                   jax.ShapeDtypeStruct((s, D_HALF), F32), jax.ShapeDtypeStruct((s, GATE_COLS), F32),
                   fshape(wa), fshape(wb), fshape(wo), fshape(fg), fshape(fg)],
        compiler_params=_params("arbitrary"),
    )(mix_a, mix_b, u_g, x, tgt, wa, wb, wo, fg)


def _lora_weight(w_up, a_up):
    z = jnp.zeros((LORA, D_HALF), w_up.dtype)
    return jnp.concatenate([jnp.concatenate([w_up, z], axis=1), jnp.concatenate([z, a_up], axis=1)], axis=0)


def _device_grads(x, tgt, p, w_a, w_up, a_up, late_weights, fwd_exchange=None, bwd_exchange=None, tail_exchange=None):
    wl = _lora_weight(w_up, a_up)
    rk = p["r_k"].reshape(1, D_HALF)
    fb = jnp.pad(p["f_bias"], ((0, 0), (0, LANES - N_HEADS)))
    qg = jnp.tile(p["q_norm_g"], (1, N_HEADS))
    kg = jnp.tile(p["k_norm_g"], (1, N_HEADS))
    fg = p["final_norm_g"].reshape(1, D_MODEL)
    mixer = (p["shift_mu"], wl, p["w0"], p["a0"], p["k_k"], p["k_a"])

    h = _rmsnorm_in(x, p["norm_g"])
    u_a = _matmul_nn(h, w_a, "inproj_rwkv")
    r, dec, k, v, av, bv, gate_a = _rwkv_prep(u_a, *mixer)
    y, st, arrived = _wkv_fwd(r, dec, k, av, bv, v, fwd_exchange)
    mix_a = _rwkv_post(y, r, k, v, gate_a, p["lnx_w"], p["lnx_b"], rk)

    w_b, w_g, w_out_a, w_out_b, w_out = late_weights(arrived)
    u_b = _matmul_nn(h, w_b, "inproj_fox")
    u_g = _matmul_nn(h, w_g, "inproj_gate")
    q, kn, vb, cc, cr = _fox_prep(u_b, fb, qg, kg)
    o, mix_b, lse = _attn_fwd(q, kn, vb, cc, cr, u_b)

    dx2, dmix_a, dmix_b, du_g, dwa, dwb, dwo, dfg, loss_vec = _merge(
        mix_a, mix_b, u_g, x, tgt, w_out_a, w_out_b, w_out, fg)

    do, dgate_b = _fox_post_bwd(dmix_b, o, u_b)
    dd = _attn_bwd_rowdot(q, kn, vb, do, lse, cc, cr)
    dq, dk_att, dv_att, dcr = _attn_bwd(q, kn, vb, do, lse, dd, cc, cr)
    dcum = jnp.pad(dcr.T, ((0, 0), (0, LANES - N_HEADS)))
    du_b, dqg, dkg, dfb = _fox_prep_bwd(u_b, dq, dk_att, dv_att, dgate_b, dcum, fb, qg, kg)
    h_t = h.T
    dw_b = _matmul_tn_acc(h_t, du_b, "dw_fox")
    dw_g = _matmul_tn_acc(h_t, du_g, "dw_gate")

    dy, dr_b, dk_b, dv_b, dgate_a, dlw, dlb, drk = _rwkv_post_bwd(
        dmix_a, y, r, k, v, gate_a, p["lnx_w"], p["lnx_b"], rk)
    scan_grads, sent = _wkv_bwd(r, dec, k, av, bv, v, dy, st,
                                bwd_exchange(dw_b, dw_g, dwa, dwb, dwo) if bwd_exchange else None)
    du_a, dmu, dwl, dw0, da0, dkkw, dkaw = _rwkv_prep_bwd(u_a, (*scan_grads, dr_b, dk_b, dv_b, dgate_a), *mixer)
    dw_a = _matmul_tn_acc(h_t, du_a, "dw_rwkv")
    dw_up, da_up = dwl[:LORA, :D_HALF], dwl[LORA:, D_HALF:]
    grad_x, dnorm_g, sent_last = _inproj_bwd(du_a, du_b, du_g, w_a, w_b, w_g, x, dx2, p["norm_g"],
                                            tail_exchange(dw_a, dw_up, da_up) if tail_exchange else None)

    grads = dict(
        norm_g=dnorm_g, w_in=(dw_a, dw_b, dw_g), shift_mu=dmu,
        w_lora_up=dw_up, w0=dw0, a_lora_up=da_up, a0=da0, k_k=dkkw, k_a=dkaw,
        r_k=drk.reshape(1, N_HEADS, HEAD), lnx_w=dlw, lnx_b=dlb, f_bias=dfb[:, :N_HEADS],
        q_norm_g=dqg.reshape(N_HEADS, HEAD).sum(axis=0, keepdims=True),
        k_norm_g=dkg.reshape(N_HEADS, HEAD).sum(axis=0, keepdims=True),
        w_out_a=dwa, w_out_b=dwb, w_out=dwo, final_norm_g=dfg.reshape(D_MODEL))
    return loss_vec, grad_x, grads, sent, sent_last


CHIP_FLIPS = ((1, 0), (0, 1), (1, 1))
ANY = pl.BlockSpec(memory_space=pl.ANY)


def _position():
    return lax.axis_index("x"), lax.axis_index("y"), lax.axis_index("c")


def _flip(v, f):
    return 1 - v if f else v


def _both(a, b):
    if a is None:
        return b
    return a if b is None else jnp.logical_and(a, b)


def _when(cond, fn):
    if cond is None:
        fn()
    else:
        pl.when(cond)(fn)


class _Moves:
    def __init__(self, send_sems, recv_sems, local_sems):
        self.send_sems, self.recv_sems, self.local_sems = send_sems, recv_sems, local_sems
        self.remote, self.local = [], []

    def send(self, src, dst, peer, landing, send_if=None, recv_if=None):
        k = len(self.remote)
        sems = dict(send_sem=self.send_sems.at[k], recv_sem=self.recv_sems.at[k], device_id=peer, device_id_type=MESH)
        out = pltpu.make_async_remote_copy(src_ref=src, dst_ref=dst, **sems)
        arrival = pltpu.make_async_remote_copy(src_ref=src, dst_ref=landing, **sems)
        self.remote.append((out, arrival, send_if, recv_if))

    def copy(self, src, dst, cond=None):
        cp = pltpu.make_async_copy(src, dst, self.local_sems.at[len(self.local)])
        self.local.append((cp, cond))

    def start(self, also=None):
        for cp, cond in self.local:
            _when(_both(also, cond), cp.start)
        for out, _, send_if, _ in self.remote:
            _when(_both(also, send_if), out.start)

    def wait_arrivals(self, also=None):
        for _, arrival, _, recv_if in self.remote:
            _when(_both(also, recv_if), arrival.wait_recv)

    def wait_sent(self, also=None):
        for out, _, send_if, _ in self.remote:
            _when(_both(also, send_if), out.wait_send)
        for cp, cond in self.local:
            _when(_both(also, cond), cp.wait)

    def wait(self, also=None):
        self.wait_arrivals(also)
        self.wait_sent(also)


class _Exchange:
    def __init__(self, operands, out_shapes, n_remote, n_local, build, n_relay=0, relay=None):
        self.operands, self.out_shapes = list(operands), list(out_shapes)
        self.n_remote, self.n_local, self.build = n_remote, n_local, build
        self.n_relay, self.relay = n_relay, relay

    def scratch(self):
        return [pltpu.SemaphoreType.DMA((self.n_remote,)), pltpu.SemaphoreType.DMA((self.n_remote,)),
                pltpu.SemaphoreType.DMA((max(self.n_local, 1),))]

    def moves(self, in_refs, out_refs, sems):
        mv = _Moves(*sems)
        self.build(mv, in_refs, out_refs)
        return mv

    def run_alone(self, name):
        n_in, n_out = len(self.operands), len(self.out_shapes)
        relay_scratch = [pltpu.SemaphoreType.DMA((self.n_relay,))] * 2 if self.relay else []

        def body(*refs):
            ins, outs, sems = refs[:n_in], refs[n_in:n_in + n_out], refs[n_in + n_out:]
            mv = self.moves(ins, outs, sems[:3])
            mv.start()
            mv.wait_arrivals()
            if self.relay:
                passed = _Moves(sems[3], sems[4], None)
                self.relay(passed, ins, outs)
                passed.start()
                passed.wait()
            mv.wait_sent()

        return pl.pallas_call(
            body, name=name, in_specs=[ANY] * n_in, out_specs=[ANY] * n_out, out_shape=self.out_shapes,
            scratch_shapes=self.scratch() + relay_scratch, compiler_params=pltpu.CompilerParams(has_side_effects=True),
        )(*self.operands)


def _is_chip(x, y, chip):
    return jnp.logical_and(x == chip // 2, y == chip % 2)


def _gather_exchange(from_chip, from_all, split=()):
    n1, n2 = len(from_chip), len(from_all)

    def rows_of(t, c):
        half = from_chip[t][1].shape[0] // 2
        return pl.ds(c * half, half)

    def build(mv, ins, outs):
        x, y, c = _position()
        me = 2 * x + y
        for t, (chip, _) in enumerate(from_chip):
            mv.copy(ins[t], outs[t], cond=_is_chip(x, y, chip))
        for t in range(n2):
            mv.copy(ins[n1 + t], outs[n1 + t].at[me])
        for fx, fy in CHIP_FLIPS:
            px, py = _flip(x, fx), _flip(y, fy)
            peer = (px, py, c)
            for t, (chip, _) in enumerate(from_chip):
                part = rows_of(t, c) if t in split else slice(None)
                mv.send(ins[t].at[part], outs[t].at[part], peer, landing=outs[t].at[part],
                        send_if=_is_chip(x, y, chip), recv_if=_is_chip(px, py, chip))
            for t in range(n2):
                mv.send(ins[n1 + t], outs[n1 + t].at[me], peer, landing=outs[n1 + t].at[2 * px + py])

    def relay(mv, ins, outs):
        x, y, c = _position()
        for t in split:
            came = jnp.logical_not(_is_chip(x, y, from_chip[t][0]))
            mv.send(outs[t].at[rows_of(t, c)], outs[t].at[rows_of(t, c)], (x, y, 1 - c),
                    landing=outs[t].at[rows_of(t, 1 - c)], send_if=came, recv_if=came)

    arrays = [a for _, a in from_chip] + list(from_all)
    shapes = [jax.ShapeDtypeStruct(a.shape, a.dtype) for _, a in from_chip]
    shapes += [jax.ShapeDtypeStruct((N_CHIPS,) + a.shape, a.dtype) for a in from_all]
    return _Exchange(arrays, shapes, len(CHIP_FLIPS) * (n1 + n2), n1 + n2, build,
                     n_relay=len(split), relay=relay if split else None)


def _scatter_exchange(to_chip, to_all):
    n1, n2 = len(to_chip), len(to_all)

    def build(mv, ins, outs):
        x, y, c = _position()
        for f, (fx, fy) in enumerate(CHIP_FLIPS):
            px, py = _flip(x, fx), _flip(y, fy)
            peer = (px, py, c)
            for t, (chip, _) in enumerate(to_chip):
                mv.send(ins[t], outs[t].at[f], peer, landing=outs[t].at[f],
                        send_if=_is_chip(px, py, chip), recv_if=_is_chip(x, y, chip))
            for t in range(n2):
                mv.send(ins[n1 + t].at[2 * px + py], outs[n1 + t].at[f], peer, landing=outs[n1 + t].at[f])

    arrays = [a for _, a in to_chip] + list(to_all)
    shapes = [jax.ShapeDtypeStruct((len(CHIP_FLIPS),) + a.shape, a.dtype) for _, a in to_chip]
    shapes += [jax.ShapeDtypeStruct((len(CHIP_FLIPS),) + a.shape[1:], a.dtype) for a in to_all]
    return _Exchange(arrays, shapes, len(CHIP_FLIPS) * (n1 + n2), 0, build)


def _swap_sibling(tensors):
    n = len(tensors)

    def body(*refs):
        ins, outs = refs[:n], refs[n:2 * n]
        send_sems, recv_sems = refs[2 * n:]
        x, y, c = _position()
        copies = [pltpu.make_async_remote_copy(
            src_ref=ins[t], dst_ref=outs[t], send_sem=send_sems.at[t], recv_sem=recv_sems.at[t],
            device_id=(x, y, 1 - c), device_id_type=MESH) for t in range(n)]
        for cp in copies:
            cp.start()
        for cp in copies:
            cp.wait_recv()
        for cp in copies:
            cp.wait_send()

    return pl.pallas_call(
        body, name="swap_sibling", in_specs=[ANY] * n, out_specs=[ANY] * n,
        out_shape=[jax.ShapeDtypeStruct(a.shape, a.dtype) for a in tensors],
        scratch_shapes=[pltpu.SemaphoreType.DMA((n,)), pltpu.SemaphoreType.DMA((n,))],
        compiler_params=pltpu.CompilerParams(has_side_effects=True),
    )(*tensors)


def _allreduce_small(slab):
    stages = 3

    def body(x_ref, o_ref, buf, send_sems, recv_sems):
        x, y, c = _position()
        peers = ((1 - x, y, c), (x, 1 - y, c), (x, y, 1 - c))
        o_ref[...] = x_ref[...]
        for k, peer in enumerate(peers):
            cp = pltpu.make_async_remote_copy(src_ref=o_ref, dst_ref=buf.at[k], send_sem=send_sems.at[k],
                                              recv_sem=recv_sems.at[k], device_id=peer, device_id_type=MESH)
            cp.start()
            cp.wait()
            o_ref[...] = o_ref[...] + buf[k]

    return pl.pallas_call(
        body, name="allreduce_small",
        in_specs=[pl.BlockSpec(memory_space=pltpu.VMEM)], out_specs=pl.BlockSpec(memory_space=pltpu.VMEM),
        out_shape=jax.ShapeDtypeStruct(slab.shape, slab.dtype),
        scratch_shapes=[pltpu.VMEM((stages,) + slab.shape, slab.dtype),
                        pltpu.SemaphoreType.DMA((stages,)), pltpu.SemaphoreType.DMA((stages,))],
        compiler_params=pltpu.CompilerParams(has_side_effects=True),
    )(slab)


def _row_tile(r):
    return min(r, 256)


def _sum4(stack, recv, me):
    _, r, c = stack.shape
    tr = _row_tile(r)

    def body(me_ref, own_ref, recv_ref, o_ref):
        o_ref[...] = (((own_ref[...] + recv_ref[0].astype(F32)) + recv_ref[1].astype(F32))
                      + recv_ref[2].astype(F32))

    return pl.pallas_call(
        body, name="sum_partials",
        grid_spec=pltpu.PrefetchScalarGridSpec(
            num_scalar_prefetch=1, grid=(r // tr,),
            in_specs=[pl.BlockSpec((None, tr, c), lambda i, me_ref: (me_ref[0], i, 0)),
                      pl.BlockSpec((len(CHIP_FLIPS), tr, c), lambda i, me_ref: (0, i, 0))],
            out_specs=pl.BlockSpec((tr, c), lambda i, me_ref: (i, 0))),
        out_shape=jax.ShapeDtypeStruct((r, c), F32), compiler_params=_params("parallel"),
    )(me, stack, recv)


def _sum_block(own, recv):
    r, c = own.shape
    tr = _row_tile(r)

    def body(own_ref, recv_ref, o_ref):
        o_ref[...] = (((own_ref[...] + recv_ref[0].astype(F32)) + recv_ref[1].astype(F32))
                      + recv_ref[2].astype(F32))

    return pl.pallas_call(
        body, name="sum_block", grid=(r // tr,),
        in_specs=[pl.BlockSpec((tr, c), lambda i: (i, 0)), pl.BlockSpec((len(CHIP_FLIPS), tr, c), lambda i: (0, i, 0))],
        out_specs=pl.BlockSpec((tr, c), lambda i: (i, 0)),
        out_shape=jax.ShapeDtypeStruct((r, c), F32), compiler_params=_params("parallel"),
    )(own, recv)


def _adamw_math(w, g, m, v):
    m = ADAM_B1 * m + (1.0 - ADAM_B1) * g
    v = ADAM_B2 * v + (1.0 - ADAM_B2) * (g * g)
    m_hat = m / (1.0 - ADAM_B1 ** ADAM_STEP)
    v_hat = v / (1.0 - ADAM_B2 ** ADAM_STEP)
    delta = -ADAM_LR * (m_hat / (jnp.sqrt(v_hat) + ADAM_EPS) + ADAM_WD * w)
    return delta, m, v


def _adamw(w, m, v, g_parts, name):
    r, c = w.shape
    tr = _row_tile(r)
    n = len(g_parts)

    def body(*refs):
        w_ref, m_ref, v_ref = refs[:3]
        g_refs = refs[3:3 + n]
        g_out, d_out, m_out, v_out = refs[3 + n:]
        g = g_refs[0][...]
        for ref in g_refs[1:]:
            g = g + ref[...]
        g_out[...] = g
        d_out[...], m_out[...], v_out[...] = _adamw_math(w_ref[...], g, m_ref[...], v_ref[...])

    blk = pl.BlockSpec((tr, c), lambda i: (i, 0))
    return pl.pallas_call(
        body, name=name, grid=(r // tr,), in_specs=[blk] * (3 + n), out_specs=[blk] * 4,
        out_shape=[jax.ShapeDtypeStruct((r, c), F32)] * 4, compiler_params=_params("parallel"),
    )(w, m, v, *g_parts)


SHARDED = ("w_in", "w_lora_up", "a_lora_up", "w_out_a", "w_out_b", "w_out")
ROW_SHARDED = ("w_out",)
SMALL = ("norm_g", "shift_mu", "w0", "a0", "k_k", "k_a", "r_k", "lnx_w", "lnx_b", "f_bias", "q_norm_g", "k_norm_g",
         "final_norm_g")
WEIGHTS = ("norm_g", "w_in", "shift_mu", "w_lora_up", "w0", "a_lora_up", "a0", "k_k", "k_a", "r_k", "lnx_w", "lnx_b",
           "f_bias", "q_norm_g", "k_norm_g", "w_out_a", "w_out_b", "w_out", "final_norm_g")
SLAB_ROWS = 16
SLAB_COLS = SEC


def _to_slab(named, extra=None):
    rows = [jnp.pad(named[n].reshape(1, -1), ((0, 0), (0, SLAB_COLS - named[n].size))) for n in SMALL]
    if extra is not None:
        rows.append(jnp.pad(extra.reshape(1, -1), ((0, 0), (0, SLAB_COLS - extra.size))))
    rows.append(jnp.zeros((SLAB_ROWS - len(rows), SLAB_COLS), F32))
    return jnp.concatenate(rows, axis=0)


def _from_slab(slab, shapes):
    return {n: slab[i, :math.prod(shapes[n])].reshape(shapes[n]) for i, n in enumerate(SMALL)}


def _by_chip(g, name):
    if name in ROW_SHARDED:
        return g.reshape(N_CHIPS, g.shape[0] // N_CHIPS, g.shape[1])
    r, c = g.shape
    return g.reshape(r, N_CHIPS, c // N_CHIPS).transpose(1, 0, 2)


def _from_chips(stack, name):
    if name in ROW_SHARDED:
        return stack.reshape(-1, stack.shape[2])
    _, r, c = stack.shape
    return stack.transpose(1, 0, 2).reshape(r, N_CHIPS * c)


def kernel(x, norm_g, w_in, shift_mu, w_lora_up, w0, a_lora_up, a0, k_k, k_a, r_k, lnx_w, lnx_b, f_bias, q_norm_g, k_norm_g, w_out_a, w_out_b, w_out, final_norm_g, loss_target, m_norm_g, m_w_in, m_shift_mu, m_w_lora_up, m_w0, m_a_lora_up, m_a0, m_k_k, m_k_a, m_r_k, m_lnx_w, m_lnx_b, m_f_bias, m_q_norm_g, m_k_norm_g, m_w_out_a, m_w_out_b, m_w_out, m_final_norm_g, v_norm_g, v_w_in, v_shift_mu, v_w_lora_up, v_w0, v_a_lora_up, v_a0, v_k_k, v_k_a, v_r_k, v_lnx_w, v_lnx_b, v_f_bias, v_q_norm_g, v_k_norm_g, v_w_out_a, v_w_out_b, v_w_out, v_final_norm_g):
    w = dict(norm_g=norm_g, w_in=w_in, shift_mu=shift_mu, w_lora_up=w_lora_up, w0=w0, a_lora_up=a_lora_up, a0=a0,
             k_k=k_k, k_a=k_a, r_k=r_k, lnx_w=lnx_w, lnx_b=lnx_b, f_bias=f_bias, q_norm_g=q_norm_g,
             k_norm_g=k_norm_g, w_out_a=w_out_a, w_out_b=w_out_b, w_out=w_out, final_norm_g=final_norm_g)
    m = dict(norm_g=m_norm_g, w_in=m_w_in, shift_mu=m_shift_mu, w_lora_up=m_w_lora_up, w0=m_w0,
             a_lora_up=m_a_lora_up, a0=m_a0, k_k=m_k_k, k_a=m_k_a, r_k=m_r_k, lnx_w=m_lnx_w, lnx_b=m_lnx_b,
             f_bias=m_f_bias, q_norm_g=m_q_norm_g, k_norm_g=m_k_norm_g, w_out_a=m_w_out_a, w_out_b=m_w_out_b,
             w_out=m_w_out, final_norm_g=m_final_norm_g)
    v = dict(norm_g=v_norm_g, w_in=v_w_in, shift_mu=v_shift_mu, w_lora_up=v_w_lora_up, w0=v_w0,
             a_lora_up=v_a_lora_up, a0=v_a0, k_k=v_k_k, k_a=v_k_a, r_k=v_r_k, lnx_w=v_lnx_w, lnx_b=v_lnx_b,
             f_bias=v_f_bias, q_norm_g=v_q_norm_g, k_norm_g=v_k_norm_g, w_out_a=v_w_out_a, w_out_b=v_w_out_b,
             w_out=v_w_out, final_norm_g=v_final_norm_g)
    shapes = {n: w[n].shape for n in WEIGHTS}

    shard = {n: w[n][0].astype(BF16) for n in SHARDED}
    late = ("w_out_a", "w_out_b", "w_out")
    loras = ("w_lora_up", "a_lora_up")
    w_in_head, w_in_tail = shard["w_in"][:, :A_TAIL], shard["w_in"][:, A_TAIL:]
    shard0, shard1_head, up_stack, aup_stack = _gather_exchange(
        [(0, shard["w_in"]), (1, w_in_head)], [shard[n] for n in loras], split=(0,)).run_alone("gather_early")
    w_a = jnp.concatenate([shard0, shard1_head], axis=1)

    def late_weights(arrived):
        shard1_tail, shard2, shard3 = arrived[:3]
        w_b = jnp.concatenate([shard1_tail, shard2[:, :B_TAIL], jnp.zeros((D_MODEL, SEC - FOX_REAL), BF16)], axis=1)
        w_g = jnp.concatenate([shard2[:, B_TAIL:], shard3], axis=1)
        return (w_b, w_g, *[_from_chips(s, n) for n, s in zip(late, arrived[3:])])

    own = {}

    def bwd_exchange(dw_b, dw_g, dwa, dwb, dwo):
        own["tail1"] = dw_b[:, :B_HEAD]
        own["block2"] = jnp.concatenate([dw_b[:, B_HEAD:FOX_REAL], dw_g[:, :G_HEAD]], axis=1)
        own["block3"] = dw_g[:, G_HEAD:]
        own.update({n: _by_chip(g, n) for n, g in zip(late, (dwa, dwb, dwo))})
        return _scatter_exchange([(1, own["tail1"].astype(BF16)), (2, own["block2"].astype(BF16)),
                                  (3, own["block3"].astype(BF16))], [own[n].astype(BF16) for n in late])

    def tail_exchange(dw_a, dw_up, da_up):
        own["block0"], own["head1"] = dw_a[:, :SHARD_COLS], dw_a[:, SHARD_COLS:]
        own.update({n: _by_chip(g, n) for n, g in zip(loras, (dw_up, da_up))})
        return _scatter_exchange([(0, own["block0"].astype(BF16)), (1, own["head1"].astype(BF16))],
                                 [own[n].astype(BF16) for n in loras])

    small = {n: w[n] for n in SMALL}
    loss_vec, grad_x, grads, sent, sent_last = _device_grads(
        x[0], loss_target[0], small, w_a, _from_chips(up_stack, "w_lora_up"), _from_chips(aup_stack, "a_lora_up"),
        late_weights, _gather_exchange([(1, w_in_tail), (2, shard["w_in"]), (3, shard["w_in"])], [shard[n] for n in late]),
        bwd_exchange, tail_exchange)

    total = _allreduce_small(_to_slab(grads, extra=loss_vec))
    loss = (0.5 / D_MODEL) * jnp.sum(total[len(SMALL)])
    slab_g, slab_d, slab_m, slab_v = _adamw(_to_slab(w), _to_slab(m), _to_slab(v), [total], "adamw_small")
    out_g, out_d, out_m, out_v = (_from_slab(s, shapes) for s in (total, slab_d, slab_m, slab_v))
    del slab_g

    xpos, ypos, _ = _position()
    me = (2 * xpos + ypos).astype(jnp.int32).reshape(1)
    core_sum = {"w_in": lax.switch(me[0], [
        lambda: _sum_block(own["block0"], sent_last[0]),
        lambda: jnp.concatenate([_sum_block(own["head1"], sent_last[1]), _sum_block(own["tail1"], sent[0])], axis=1),
        lambda: _sum_block(own["block2"], sent[1]),
        lambda: _sum_block(own["block3"], sent[2])])}
    core_sum.update({n: _sum4(own[n], r, me) for n, r in zip(late, sent[3:])})
    core_sum.update({n: _sum4(own[n], r, me) for n, r in zip(loras, sent_last[2:])})
    sibling_sums = _swap_sibling([core_sum[n] for n in SHARDED])
    for n, theirs in zip(SHARDED, sibling_sums):
        g, d, m2, v2 = _adamw(w[n][0], m[n][0], v[n][0], [core_sum[n], theirs], "adamw_" + n)
        out_g[n], out_d[n], out_m[n], out_v[n] = (a.reshape(shapes[n]) for a in (g, d, m2, v2))

    return (loss, grad_x.reshape(x.shape), *[out_g[n] for n in WEIGHTS], *[out_d[n] for n in WEIGHTS],
            *[out_m[n] for n in WEIGHTS], *[out_v[n] for n in WEIGHTS])
```

```python
import functools
import math

import jax
import jax.numpy as jnp
from jax import lax
from jax.experimental import pallas as pl
from jax.experimental.pallas import tpu as pltpu
from jax.experimental.pallas import tpu_sc as plsc

F32 = jnp.float32
BF16 = jnp.bfloat16

D_MODEL = 1024
D_HALF = 512
HEAD = 64
N_HEADS = 8
LORA = 64
RWKV_COLS = 2176
FOX_REAL = 2056
SEC = 2176
GATE_COLS = 2048
IN_COLS = 6280
N_CHIPS = 4
SHARD_COLS = IN_COLS // N_CHIPS
A_TAIL = RWKV_COLS - SHARD_COLS
B_HEAD = SHARD_COLS - A_TAIL
B_TAIL = FOX_REAL - B_HEAD
G_HEAD = SHARD_COLS - B_TAIL
RMS_EPS = 1e-6
LNX_EPS = 64e-5
ATT_SCALE = HEAD ** -0.5
NEG = -1e30

ADAM_LR = 0.001
ADAM_B1 = 0.9
ADAM_B2 = 0.999
ADAM_EPS = 1e-08
ADAM_WD = 0.01
ADAM_STEP = 10

LANES = 128
SUBLANES = 8
VMEM_LIMIT = 56 * 1024 * 1024
MESH = pl.DeviceIdType.MESH


def _params(*sem):
    return pltpu.CompilerParams(dimension_semantics=sem if sem else None, vmem_limit_bytes=VMEM_LIMIT)


def _sigmoid(x):
    return 1.0 / (1.0 + jnp.exp(-x))


def _log_sigmoid(x):
    return jnp.minimum(x, 0.0) - jnp.log(1.0 + jnp.exp(-jnp.abs(x)))


def _head_ones():
    r = lax.broadcasted_iota(jnp.int32, (LANES, LANES), 0) >> 6
    c = lax.broadcasted_iota(jnp.int32, (LANES, LANES), 1) >> 6
    return (r == c).astype(BF16)


def _split3(x):
    hi = x.astype(BF16)
    r1 = x - hi.astype(F32)
    mid = r1.astype(BF16)
    lo = (r1 - mid.astype(F32)).astype(BF16)
    return hi, mid, lo


def _exact_dot(x, ones_bf16, ones_first=False):
    out = None
    for piece in _split3(x):
        if ones_first:
            t = jnp.dot(ones_bf16, piece, preferred_element_type=F32)
        else:
            t = jnp.dot(piece, ones_bf16, preferred_element_type=F32)
        out = t if out is None else out + t
    return out


def _head_sum(x, bd):
    n = x.shape[1] // LANES
    parts = [_exact_dot(x[:, i * LANES:(i + 1) * LANES], bd) for i in range(n)]
    return parts[0] if n == 1 else jnp.concatenate(parts, axis=1)


def _dot_nt(a, b):
    return lax.dot_general(a, b, (((1,), (1,)), ((), ())), preferred_element_type=F32)


def _dot_tn(a, b):
    return lax.dot_general(a, b, (((0,), (0,)), ((), ())), preferred_element_type=F32)


def _colsum(x):
    return jnp.sum(x, axis=0, keepdims=True)


def _rmsnorm_in(x, g, tm=512):
    s, d = x.shape

    def body(x_ref, g_ref, h_ref):
        xv = x_ref[...]
        r = lax.rsqrt(jnp.mean(xv * xv, axis=-1, keepdims=True) + RMS_EPS)
        h_ref[...] = (xv * r * g_ref[...]).astype(BF16)

    return pl.pallas_call(
        body, name="rmsnorm_in", grid=(s // tm,),
        in_specs=[pl.BlockSpec((tm, d), lambda i: (i, 0)), pl.BlockSpec((1, d), lambda i: (0, 0))],
        out_specs=pl.BlockSpec((tm, d), lambda i: (i, 0)),
        out_shape=jax.ShapeDtypeStruct((s, d), BF16), compiler_params=_params("parallel"),
    )(x, g)


def _matmul_nn(a, b, name, tm=512):
    m, k = a.shape
    n = b.shape[1]

    def body(a_ref, b_ref, o_ref):
        o_ref[...] = jnp.dot(a_ref[...], b_ref[...], preferred_element_type=F32)

    return pl.pallas_call(
        body, name=name, grid=(m // tm,),
        in_specs=[pl.BlockSpec((tm, k), lambda i: (i, 0)), pl.BlockSpec((k, n), lambda i: (0, 0))],
        out_specs=pl.BlockSpec((tm, n), lambda i: (i, 0)),
        out_shape=jax.ShapeDtypeStruct((m, n), F32), compiler_params=_params("parallel"),
    )(a, b)


def _matmul_tn_acc(at, b, name, tk=512):
    m, k = at.shape
    n = b.shape[1]

    def body(a_ref, b_ref, o_ref):
        j = pl.program_id(0)

        @pl.when(j == 0)
        def _():
            o_ref[...] = jnp.zeros_like(o_ref)

        o_ref[...] += jnp.dot(a_ref[...], b_ref[...].astype(BF16), preferred_element_type=F32)

    return pl.pallas_call(
        body, name=name, grid=(k // tk,),
        in_specs=[pl.BlockSpec((m, tk), lambda j: (0, j)), pl.BlockSpec((tk, n), lambda j: (j, 0))],
        out_specs=pl.BlockSpec((m, n), lambda j: (0, 0)),
        out_shape=jax.ShapeDtypeStruct((m, n), F32), compiler_params=_params("arbitrary"),
    )(at, b)


def _inproj_bwd(du_a, du_b, du_g, w_a, w_b, w_g, x, dx2, g, exchange=None, tm=256):
    s, d = x.shape
    nb = s // tm

    def body(*refs):
        ((da_ref, db_ref, dg_ref, wa_ref, wb_ref, wg_ref, x_ref, dx2_ref, g_ref), (gx_ref, gg_ref), _,
         moves) = _split_refs(refs, 9, 2, exchange)
        i = pl.program_id(0)
        if moves:
            moves.start(also=(i == 0))

        @pl.when(i == 0)
        def _():
            gg_ref[...] = jnp.zeros_like(gg_ref)

        dh = _dot_nt(da_ref[...].astype(BF16), wa_ref[...])
        dh += _dot_nt(db_ref[...].astype(BF16), wb_ref[...])
        dh += _dot_nt(dg_ref[...].astype(BF16), wg_ref[...])
        xv = x_ref[...]
        r = lax.rsqrt(jnp.mean(xv * xv, axis=-1, keepdims=True) + RMS_EPS)
        xh = xv * r
        gg_ref[...] += _colsum(dh * xh)
        dxh = dh * g_ref[...]
        gx_ref[...] = dx2_ref[...] + r * (dxh - xh * jnp.mean(dxh * xh, axis=-1, keepdims=True))
        if moves:
            moves.wait(also=(i == nb - 1))

    row = lambda w: pl.BlockSpec((tm, w), lambda i: (i, 0))
    full = lambda a: pl.BlockSpec(a.shape, lambda i: (0, 0))
    ex_in = exchange.operands if exchange else []
    ex_out = exchange.out_shapes if exchange else []
    res = pl.pallas_call(
        body, name="inproj_bwd", grid=(nb,),
        in_specs=[row(SEC), row(SEC), row(GATE_COLS), full(w_a), full(w_b), full(w_g), row(d), row(d), full(g)]
                 + [ANY] * len(ex_in),
        out_specs=[row(d), pl.BlockSpec((1, d), lambda i: (0, 0))] + [ANY] * len(ex_out),
        out_shape=[jax.ShapeDtypeStruct((s, d), F32), jax.ShapeDtypeStruct((1, d), F32)] + ex_out,
        scratch_shapes=exchange.scratch() if exchange else [],
        compiler_params=_params("arbitrary"),
    )(du_a, du_b, du_g, w_a, w_b, w_g, x, dx2, g, *ex_in)
    return res[0], res[1], list(res[2:])


def _rwkv_elementwise(ua, prev_row, first, mu, wl, w0, a0, kkw, kaw, bd):
    tm = ua.shape[0]
    rows = lax.broadcasted_iota(jnp.int32, (tm, 1), 0)
    prev = jnp.where(first, jnp.zeros_like(prev_row), prev_row)
    shifted = jnp.where(rows == 0, prev, pltpu.roll(ua, 1, 0))
    delta = shifted - ua
    us = ua + delta * mu
    r = us[:, 0:512]
    k0 = us[:, 512:1024]
    v = us[:, 1024:1536]
    lo = us[:, 1536:1664]
    gate = us[:, 1664:2176]
    lane = lax.broadcasted_iota(jnp.int32, (1, LANES), 1)
    th = jnp.tanh(lo)
    lin = jnp.where(lane < LORA, th, lo)
    ll = jnp.dot(lin.astype(BF16), wl, preferred_element_type=F32)
    sz = _sigmoid(w0 + ll[:, :512])
    e = sz * math.exp(-0.5)
    dec = jnp.exp(-e)
    a = _sigmoid(a0 + ll[:, 512:])
    kk0 = k0 * kkw
    ss = _head_sum(kk0 * kk0, bd)
    nrm = jnp.maximum(jnp.sqrt(ss), 1e-12)
    kk = kk0 / nrm
    k = k0 * (1.0 + (a - 1.0) * kaw)
    return dict(delta=delta, us=us, r=r, k0=k0, v=v, lo=lo, gate=gate, th=th, lin=lin, sz=sz, e=e, dec=dec,
                a=a, kk0=kk0, ss=ss, nrm=nrm, kk=kk, k=k)


def _rwkv_prep(u_a, mu, wl, w0, a0, kkw, kaw, tm=256):
    s = u_a.shape[0]

    def body(ua_ref, prev_ref, mu_ref, wl_ref, w0_ref, a0_ref, kkw_ref, kaw_ref,
             r_ref, w_ref, k_ref, v_ref, a_ref, b_ref, g_ref):
        i = pl.program_id(0)
        f = _rwkv_elementwise(ua_ref[...], prev_ref[7:8, :], i == 0, mu_ref[...], wl_ref[...], w0_ref[...],
                              a0_ref[...], kkw_ref[...], kaw_ref[...], _head_ones())
        r_ref[...] = f["r"]
        w_ref[...] = f["dec"]
        k_ref[...] = f["k"]
        v_ref[...] = f["v"]
        a_ref[...] = -f["kk"]
        b_ref[...] = f["kk"] * f["a"]
        g_ref[...] = f["gate"]

    vec = lambda w: pl.BlockSpec((1, w), lambda i: (0, 0))
    out = pl.BlockSpec((tm, D_HALF), lambda i: (i, 0))
    return pl.pallas_call(
        body, name="rwkv_prep", grid=(s // tm,),
        in_specs=[pl.BlockSpec((tm, SEC), lambda i: (i, 0)),
                  pl.BlockSpec((8, SEC), lambda i: (jnp.maximum(i * (tm // 8) - 1, 0), 0)),
                  vec(SEC), pl.BlockSpec((LANES, 2 * D_HALF), lambda i: (0, 0)),
                  vec(D_HALF), vec(D_HALF), vec(D_HALF), vec(D_HALF)],
        out_specs=[out] * 7,
        out_shape=[jax.ShapeDtypeStruct((s, D_HALF), F32)] * 7,
        compiler_params=_params("parallel"),
    )(u_a, u_a, mu, wl, w0, a0, kkw, kaw)


SCAN_TB = 128
N_PAIRS = 4


def _pair_sum(x, left):
    s_l = jnp.sum(jnp.where(left, x, 0.0), axis=1, keepdims=True)
    s_r = jnp.sum(jnp.where(left, 0.0, x), axis=1, keepdims=True)
    return jnp.where(left, s_l, s_r)


def _quad_consts():
    lane = lax.broadcasted_iota(jnp.int32, (HEAD, 2 * LANES), 1)
    rowi = lax.broadcasted_iota(jnp.int32, (HEAD, 2 * LANES), 0)
    diag2 = rowi == (lane & (HEAD - 1))
    r = lax.broadcasted_iota(jnp.int32, (2 * LANES, 2 * LANES), 0) >> 6
    c = lax.broadcasted_iota(jnp.int32, (2 * LANES, 2 * LANES), 1) >> 6
    return diag2, (r == c).astype(BF16)


def _rows_to_columns(x8, diag2, bd2):
    lhs = jnp.concatenate([jnp.where(diag2, x8[i:i + 1], 0.0).astype(BF16) for i in range(SUBLANES)], axis=0)
    return jnp.dot(lhs, bd2, preferred_element_type=F32)


def _diag_rows(qtile, diag2, bd2, sub_row2):
    res = jnp.dot(qtile, bd2, preferred_element_type=F32)
    out = jnp.zeros((SUBLANES, 2 * LANES), F32)
    for i in range(SUBLANES):
        out = jnp.where(sub_row2 == i, _colsum(jnp.where(diag2, res[i * HEAD:(i + 1) * HEAD], 0.0)), out)
    return out


def _store_tile(qbuf, slot, p, i, x):
    qbuf[slot, p // 2, i * HEAD:(i + 1) * HEAD, (p % 2) * LANES:(p % 2 + 1) * LANES] = x.astype(BF16)


def _left_half():
    return lax.broadcasted_iota(jnp.int32, (HEAD, LANES), 1) < HEAD


def _split_refs(refs, n_rows, n_out, exchange):
    n_in = len(exchange.operands) if exchange else 0
    n_ex_out = len(exchange.out_shapes) if exchange else 0
    refs = list(refs)
    rows, refs = refs[:n_rows], refs[n_rows:]
    ex_in, refs = refs[:n_in], refs[n_in:]
    outs, refs = refs[:n_out], refs[n_out:]
    ex_out, refs = refs[:n_ex_out], refs[n_ex_out:]
    scratch, sems = (refs[:-3], refs[-3:]) if exchange else (refs, None)
    moves = exchange.moves(ex_in, ex_out, sems) if exchange else None
    return rows, outs, scratch, moves


def _wkv_fwd(r, w, k, a, b, v, exchange=None):
    s = r.shape[0]
    tb = SCAN_TB
    nb = s // tb

    def body(*refs):
        (r_ref, w_ref, k_ref, a_ref, b_ref, v_ref), (y_ref, st_ref), (state, vbuf, qbuf), moves = _split_refs(
            refs, 6, 2, exchange)
        g = pl.program_id(0)
        if moves:
            moves.start(also=(g == 0))

        @pl.when(g == 0)
        def _():
            state[...] = jnp.zeros_like(state)
            qbuf[...] = jnp.zeros_like(qbuf)

        left = _left_half()
        diag2, bd2 = _quad_consts()
        sub_row2 = lax.broadcasted_iota(jnp.int32, (SUBLANES, 2 * LANES), 0)
        groups = tb // SUBLANES
        quads = [slice(g2 * 2 * LANES, (g2 + 1) * 2 * LANES) for g2 in range(2)]

        def rows_of(q):
            return pl.ds(pl.multiple_of(q * SUBLANES, SUBLANES), SUBLANES)

        def v_tiles(q, slot):
            v8 = v_ref[rows_of(q), :]
            for g2 in range(2):
                vbuf[slot, g2] = _rows_to_columns(v8[:, quads[g2]], diag2, bd2)

        def chain(q, slot):
            rows8 = rows_of(q)
            a8, w8, b8, k8, r8 = (x[rows8, :] for x in (a_ref, w_ref, b_ref, k_ref, r_ref))
            pairs = [slice(p * LANES, (p + 1) * LANES) for p in range(N_PAIRS)]
            a_next = pltpu.roll(a8, SUBLANES - 1, 0)
            wa8 = w8 * a_next
            ba8 = jnp.concatenate([_pair_sum(b8[:, pr] * a_next[:, pr], left[0:SUBLANES]) for pr in pairs], axis=1)
            ka8 = jnp.concatenate([_pair_sum(k8[:, pr] * a_next[:, pr], left[0:SUBLANES]) for pr in pairs], axis=1)
            sp = [state[p] for p in range(N_PAIRS)]
            for i in range(0, SUBLANES, 2):
                r0, r1 = slice(i, i + 1), slice(i + 1, i + 2)
                sums = [(_pair_sum(sp[p] * a8[r0, pairs[p]], left), _pair_sum(sp[p] * wa8[r0, pairs[p]], left))
                        for p in range(N_PAIRS)]
                sa0, sa1 = [s[0] for s in sums], [s[1] for s in sums]
                for p in range(N_PAIRS):
                    pr = pairs[p]
                    inner = slice((p % 2) * LANES, (p % 2 + 1) * LANES)
                    vt0 = vbuf[slot, p // 2, i * HEAD:(i + 1) * HEAD, inner]
                    vt1 = vbuf[slot, p // 2, (i + 1) * HEAD:(i + 2) * HEAD, inner]
                    sa_next = sa1[p] + sa0[p] * ba8[r0, pr] + vt0 * ka8[r0, pr]
                    s1 = sp[p] * w8[r0, pr] + sa0[p] * b8[r0, pr] + vt0 * k8[r0, pr]
                    st_ref[q * SUBLANES + i, p] = s1
                    _store_tile(qbuf, slot, p, i, s1 * r8[r0, pr])
                    s2 = s1 * w8[r1, pr] + sa_next * b8[r1, pr] + vt1 * k8[r1, pr]
                    st_ref[q * SUBLANES + i + 1, p] = s2
                    _store_tile(qbuf, slot, p, i + 1, s2 * r8[r1, pr])
                    sp[p] = s2
            for p in range(N_PAIRS):
                state[p] = sp[p]

        def y_rows(q, slot):
            for g2 in range(2):
                y_ref[rows_of(q), quads[g2]] = _diag_rows(qbuf[slot, g2], diag2, bd2, sub_row2)

        v_tiles(0, 0)

        def two_groups(j, carry):
            q0 = 2 * j
            v_tiles(q0 + 1, 1)
            chain(q0, 0)
            y_rows(jnp.maximum(q0 - 1, 0), 1)
            v_tiles(jnp.minimum(q0 + 2, groups - 1), 0)
            chain(q0 + 1, 1)
            y_rows(q0, 0)
            return carry

        lax.fori_loop(0, groups // 2, two_groups, 0)
        y_rows(groups - 1, 1)
        if moves:
            moves.wait(also=(g == nb - 1))

    rows = pl.BlockSpec((tb, D_HALF), lambda g: (g, 0))
    ex_in = exchange.operands if exchange else []
    ex_out = exchange.out_shapes if exchange else []
    res = pl.pallas_call(
        body, name="wkv_fwd", grid=(nb,),
        in_specs=[rows] * 6 + [ANY] * len(ex_in),
        out_specs=[rows, pl.BlockSpec((tb, N_PAIRS, HEAD, LANES), lambda g: (g, 0, 0, 0))] + [ANY] * len(ex_out),
        out_shape=[jax.ShapeDtypeStruct((s, D_HALF), F32),
                   jax.ShapeDtypeStruct((s, N_PAIRS, HEAD, LANES), F32)] + ex_out,
        scratch_shapes=[pltpu.VMEM((N_PAIRS, HEAD, LANES), F32),
                        pltpu.VMEM((2, 2, SUBLANES * HEAD, 2 * LANES), F32),
                        pltpu.VMEM((2, 2, SUBLANES * HEAD, 2 * LANES), BF16)]
                       + (exchange.scratch() if exchange else []),
        compiler_params=_params("arbitrary"),
    )(r, w, k, a, b, v, *ex_in)
    return res[0], res[1], list(res[2:])


def _wkv_bwd(r, w, k, a, b, v, dy, st, exchange=None):
    s = r.shape[0]
    tb = SCAN_TB
    nb = s // tb

    def body(*refs):
        ((r_ref, w_ref, k_ref, a_ref, b_ref, v_ref, dy_ref, st_ref, before_ref),
         (dr_ref, dw_ref, dk_ref, dv_ref, da_ref, db_ref), (dstate, vbuf, qbuf), moves) = _split_refs(refs, 9, 6, exchange)
        g = pl.program_id(0)
        first_block = g == nb - 1
        if moves:
            moves.start(also=(g == 0))

        @pl.when(g == 0)
        def _():
            dstate[...] = jnp.zeros_like(dstate)
            qbuf[...] = jnp.zeros_like(qbuf)

        left = _left_half()
        diag2, bd2 = _quad_consts()
        sub_row = lax.broadcasted_iota(jnp.int32, (SUBLANES, LANES), 0)
        sub_row2 = lax.broadcasted_iota(jnp.int32, (SUBLANES, 2 * LANES), 0)
        groups = tb // SUBLANES
        quads = [slice(g2 * 2 * LANES, (g2 + 1) * 2 * LANES) for g2 in range(2)]
        row_refs = (dr_ref, dw_ref, dk_ref, da_ref, db_ref)

        def rows_of(q):
            return pl.ds(pl.multiple_of(q * SUBLANES, SUBLANES), SUBLANES)

        def column_tiles(q, slot):
            rows8 = rows_of(q)
            for kind, ref in enumerate((v_ref, dy_ref)):
                x8 = ref[rows8, :]
                for g2 in range(2):
                    vbuf[slot, kind, g2] = _rows_to_columns(x8[:, quads[g2]], diag2, bd2)

        def chain(q, slot):
            rows8 = rows_of(q)
            a8, w8, b8, k8, r8 = (x[rows8, :] for x in (a_ref, w_ref, b_ref, k_ref, r_ref))
            dsp = [dstate[p] for p in range(N_PAIRS)]
            outs = [[jnp.zeros((SUBLANES, LANES), F32) for _ in row_refs] for _ in range(N_PAIRS)]
            after = [st_ref[q * SUBLANES + SUBLANES - 1, p] for p in range(N_PAIRS)]
            for i in reversed(range(SUBLANES)):
                row = slice(i, i + 1)
                pl_ = [slice(p * LANES, (p + 1) * LANES) for p in range(N_PAIRS)]
                tile = [(p // 2, slice(i * HEAD, (i + 1) * HEAD), slice((p % 2) * LANES, (p % 2 + 1) * LANES))
                        for p in range(N_PAIRS)]
                if i > 0:
                    sp = [st_ref[q * SUBLANES + i - 1, p] for p in range(N_PAIRS)]
                else:
                    sp = [jnp.where(q == 0, jnp.where(first_block, 0.0, before_ref[0, p]),
                                    st_ref[jnp.maximum(q * SUBLANES - 1, 0), p]) for p in range(N_PAIRS)]
                dyt = [vbuf[(slot, 1) + tile[p]] for p in range(N_PAIRS)]
                ds = [dsp[p] + dyt[p] * r8[row, pl_[p]] for p in range(N_PAIRS)]
                dsa = [_pair_sum(ds[p] * b8[row, pl_[p]], left) for p in range(N_PAIRS)]
                sa = [_pair_sum(sp[p] * a8[row, pl_[p]], left) for p in range(N_PAIRS)]
                for p in range(N_PAIRS):
                    ar, wr, br, kr = (x[row, pl_[p]] for x in (a8, w8, b8, k8))
                    vt = vbuf[(slot, 0) + tile[p]]
                    dsp[p] = ds[p] * wr + dsa[p] * ar
                    new = (_colsum(after[p] * dyt[p]), _colsum(ds[p] * sp[p]), _colsum(ds[p] * vt),
                           _colsum(sp[p] * dsa[p]), _colsum(ds[p] * sa[p]))
                    outs[p] = [jnp.where(sub_row == i, n, o) for n, o in zip(new, outs[p])]
                    _store_tile(qbuf, slot, p, i, ds[p] * kr)
                after = sp
            for p in range(N_PAIRS):
                dstate[p] = dsp[p]
                for ref, o in zip(row_refs, outs[p]):
                    ref[rows8, p * LANES:(p + 1) * LANES] = o

        def dv_rows(q, slot):
            for g2 in range(2):
                dv_ref[rows_of(q), quads[g2]] = _diag_rows(qbuf[slot, g2], diag2, bd2, sub_row2)

        column_tiles(groups - 1, 0)

        def two_groups(j, carry):
            q0 = groups - 1 - 2 * j
            column_tiles(q0 - 1, 1)
            chain(q0, 0)
            dv_rows(jnp.minimum(q0 + 1, groups - 1), 1)
            column_tiles(jnp.maximum(q0 - 2, 0), 0)
            chain(q0 - 1, 1)
            dv_rows(q0, 0)
            return carry

        lax.fori_loop(0, groups // 2, two_groups, 0)
        dv_rows(0, 1)
        if moves:
            moves.wait(also=(g == nb - 1))

    rows = pl.BlockSpec((tb, D_HALF), lambda g: (nb - 1 - g, 0))
    ex_in = exchange.operands if exchange else []
    ex_out = exchange.out_shapes if exchange else []
    res = pl.pallas_call(
        body, name="wkv_bwd", grid=(nb,),
        in_specs=[rows] * 7 + [pl.BlockSpec((tb, N_PAIRS, HEAD, LANES), lambda g: (nb - 1 - g, 0, 0, 0)),
                               pl.BlockSpec((1, N_PAIRS, HEAD, LANES),
                                            lambda g: (jnp.maximum((nb - 1 - g) * tb - 1, 0), 0, 0, 0))]
                 + [ANY] * len(ex_in),
        out_specs=[rows] * 6 + [ANY] * len(ex_out),
        out_shape=[jax.ShapeDtypeStruct((s, D_HALF), F32)] * 6 + ex_out,
        scratch_shapes=[pltpu.VMEM((N_PAIRS, HEAD, LANES), F32),
                        pltpu.VMEM((2, 2, 2, SUBLANES * HEAD, 2 * LANES), F32),
                        pltpu.VMEM((2, 2, SUBLANES * HEAD, 2 * LANES), BF16)]
                       + (exchange.scratch() if exchange else []),
        compiler_params=_params("arbitrary"),
    )(r, w, k, a, b, v, dy, st, st, *ex_in)
    return list(res[:6]), list(res[6:])


def _rwkv_post_math(y, r, k, v, gate, lw, lb, rk, bd):
    mean = _head_sum(y, bd) * (1.0 / HEAD)
    yc = y - mean
    var = _head_sum(yc * yc, bd) * (1.0 / HEAD)
    rstd = lax.rsqrt(var + LNX_EPS)
    yn = yc * rstd
    rkk = _head_sum(r * k * rk, bd)
    sg = _sigmoid(gate)
    pre = yn * lw + lb + rkk * v
    return yn, rstd, rkk, sg, pre


def _rwkv_post(y, r, k, v, gate, lw, lb, rk, tm=256):
    s = y.shape[0]

    def body(y_ref, r_ref, k_ref, v_ref, g_ref, lw_ref, lb_ref, rk_ref, o_ref):
        gate_v = g_ref[...]
        _, _, _, sg, pre = _rwkv_post_math(y_ref[...], r_ref[...], k_ref[...], v_ref[...], gate_v,
                                           lw_ref[...], lb_ref[...], rk_ref[...], _head_ones())
        o_ref[...] = pre * (gate_v * sg)

    blk = pl.BlockSpec((tm, D_HALF), lambda i: (i, 0))
    vec = pl.BlockSpec((1, D_HALF), lambda i: (0, 0))
    return pl.pallas_call(
        body, name="rwkv_post", grid=(s // tm,),
        in_specs=[blk] * 5 + [vec] * 3, out_specs=blk,
        out_shape=jax.ShapeDtypeStruct((s, D_HALF), F32), compiler_params=_params("parallel"),
    )(y, r, k, v, gate, lw, lb, rk)


def _rwkv_post_bwd(dmix, y, r, k, v, gate, lw, lb, rk, tm=256):
    s = y.shape[0]

    def body(dm_ref, y_ref, r_ref, k_ref, v_ref, g_ref, lw_ref, lb_ref, rk_ref,
             dy_ref, dr_ref, dk_ref, dv_ref, dg_ref, dlw_ref, dlb_ref, drk_ref):
        i = pl.program_id(0)

        @pl.when(i == 0)
        def _():
            dlw_ref[...] = jnp.zeros_like(dlw_ref)
            dlb_ref[...] = jnp.zeros_like(dlb_ref)
            drk_ref[...] = jnp.zeros_like(drk_ref)

        bd = _head_ones()
        rv, kv, vv, gate_v, lw_v, rk_v = r_ref[...], k_ref[...], v_ref[...], g_ref[...], lw_ref[...], rk_ref[...]
        yn, rstd, rkk, sg, pre = _rwkv_post_math(y_ref[...], rv, kv, vv, gate_v, lw_v, lb_ref[...], rk_v, bd)
        dm = dm_ref[...]
        dg_ref[...] = dm * pre * (sg * (1.0 + gate_v * (1.0 - sg)))
        dpre = dm * (gate_v * sg)
        dlw_ref[...] += _colsum(dpre * yn)
        dlb_ref[...] += _colsum(dpre)
        dyn = dpre * lw_v
        m1 = _head_sum(dyn, bd) * (1.0 / HEAD)
        m2 = _head_sum(dyn * yn, bd) * (1.0 / HEAD)
        dy_ref[...] = rstd * (dyn - m1 - yn * m2)
        dv_ref[...] = dpre * rkk
        drkk = _head_sum(dpre * vv, bd)
        dr_ref[...] = drkk * kv * rk_v
        dk_ref[...] = drkk * rv * rk_v
        drk_ref[...] += _colsum(drkk * rv * kv)

    blk = pl.BlockSpec((tm, D_HALF), lambda i: (i, 0))
    vec = pl.BlockSpec((1, D_HALF), lambda i: (0, 0))
    return pl.pallas_call(
        body, name="rwkv_post_bwd", grid=(s // tm,),
        in_specs=[blk] * 6 + [vec] * 3, out_specs=[blk] * 5 + [vec] * 3,
        out_shape=[jax.ShapeDtypeStruct((s, D_HALF), F32)] * 5 + [jax.ShapeDtypeStruct((1, D_HALF), F32)] * 3,
        compiler_params=_params("arbitrary"),
    )(dmix, y, r, k, v, gate, lw, lb, rk)


def _rwkv_prep_bwd(u_a, grads, mu, wl, w0, a0, kkw, kaw, tm=256):
    s = u_a.shape[0]
    nb = s // tm

    def body(ua_ref, prev_ref, drs_ref, dws_ref, dks_ref, dvs_ref, das_ref, dbs_ref, drb_ref, dkb_ref, dvb_ref,
             dgt_ref, mu_ref, wl_ref, w0_ref, a0_ref, kkw_ref, kaw_ref,
             du_ref, dmu_ref, dwl_ref, dw0_ref, da0_ref, dkkw_ref, dkaw_ref, carry):
        i = pl.program_id(0)

        @pl.when(i == 0)
        def _():
            carry[...] = jnp.zeros_like(carry)
            for ref in (dmu_ref, dwl_ref, dw0_ref, da0_ref, dkkw_ref, dkaw_ref):
                ref[...] = jnp.zeros_like(ref)

        bd = _head_ones()
        mu_v, wl_v, kkw_v, kaw_v = mu_ref[...], wl_ref[...], kkw_ref[...], kaw_ref[...]
        f = _rwkv_elementwise(ua_ref[...], prev_ref[7:8, :], i == nb - 1, mu_v, wl_v, w0_ref[...],
                              a0_ref[...], kkw_v, kaw_v, bd)
        a, kk, k0 = f["a"], f["kk"], f["k0"]
        dk = dks_ref[...] + dkb_ref[...]
        dbs = dbs_ref[...]
        dkk = dbs * a - das_ref[...]
        da = dbs * kk + dk * k0 * kaw_v
        dk0 = dk * (1.0 + (a - 1.0) * kaw_v)
        dkaw_ref[...] += _colsum(dk * k0 * (a - 1.0))
        inv = 1.0 / f["nrm"]
        proj = _head_sum(dkk * kk, bd)
        dkk0 = jnp.where(f["ss"] > 1e-24, (dkk - kk * proj) * inv, dkk * inv)
        dk0 = dk0 + dkk0 * kkw_v
        dkkw_ref[...] += _colsum(dkk0 * k0)
        dza = da * a * (1.0 - a)
        da0_ref[...] += _colsum(dza)
        dz = -dws_ref[...] * f["dec"] * f["e"] * (1.0 - f["sz"])
        dw0_ref[...] += _colsum(dz)
        dll = jnp.concatenate([dz, dza], axis=1).astype(BF16)
        dwl_ref[...] += _dot_tn(f["lin"].astype(BF16), dll)
        dlin = _dot_nt(dll, wl_v)
        lane = lax.broadcasted_iota(jnp.int32, (1, LANES), 1)
        th = f["th"]
        dlo = jnp.where(lane < LORA, dlin * (1.0 - th * th), dlin)
        dus = jnp.concatenate([drs_ref[...] + drb_ref[...], dk0, dvs_ref[...] + dvb_ref[...], dlo, dgt_ref[...]],
                              axis=1)
        dmu_ref[...] += _colsum(dus * f["delta"])
        g1 = dus * mu_v
        rows = lax.broadcasted_iota(jnp.int32, (tm, 1), 0)
        up = jnp.where(rows == tm - 1, carry[...], pltpu.roll(g1, tm - 1, 0))
        du_ref[...] = dus - g1 + up
        carry[...] = g1[0:1, :]

    rev = lambda w: pl.BlockSpec((tm, w), lambda i: (nb - 1 - i, 0))
    vec = lambda w: pl.BlockSpec((1, w), lambda i: (0, 0))
    wl_spec = pl.BlockSpec((LANES, 2 * D_HALF), lambda i: (0, 0))
    return pl.pallas_call(
        body, name="rwkv_prep_bwd", grid=(nb,),
        in_specs=[rev(SEC), pl.BlockSpec((8, SEC), lambda i: (jnp.maximum((nb - 1 - i) * (tm // 8) - 1, 0), 0))]
                 + [rev(D_HALF)] * 10 + [vec(SEC), wl_spec] + [vec(D_HALF)] * 4,
        out_specs=[rev(SEC), vec(SEC), wl_spec] + [vec(D_HALF)] * 4,
        out_shape=[jax.ShapeDtypeStruct((s, SEC), F32), jax.ShapeDtypeStruct((1, SEC), F32),
                   jax.ShapeDtypeStruct((LANES, 2 * D_HALF), F32)] + [jax.ShapeDtypeStruct((1, D_HALF), F32)] * 4,
        scratch_shapes=[pltpu.VMEM((1, SEC), F32)],
        compiler_params=_params("arbitrary"),
    )(u_a, u_a, *grads, mu, wl, w0, a0, kkw, kaw)


def _tri(tm, lower):
    r = lax.broadcasted_iota(jnp.int32, (tm, tm), 0)
    c = lax.broadcasted_iota(jnp.int32, (tm, tm), 1)
    return ((r >= c) if lower else (r <= c)).astype(BF16)


def _head_rms(x, g, bd):
    rinv = lax.rsqrt(_head_sum(x * x, bd) * (1.0 / HEAD) + RMS_EPS)
    xh = x * rinv
    return xh, rinv, xh * g


def _fox_prep(u_b, fb, qg, kg, tm=256):
    s = u_b.shape[0]

    def body(ub_ref, fb_ref, qg_ref, kg_ref, q_ref, k_ref, v_ref, cc_ref, cr_ref, carry):
        i = pl.program_id(0)

        @pl.when(i == 0)
        def _():
            carry[...] = jnp.zeros_like(carry)

        bd = _head_ones()
        _, _, qn = _head_rms(ub_ref[:, 0:512], qg_ref[...], bd)
        _, _, kn = _head_rms(ub_ref[:, 512:1024], kg_ref[...], bd)
        q_ref[...] = (qn * ATT_SCALE).astype(BF16)
        k_ref[...] = kn.astype(BF16)
        v_ref[...] = ub_ref[:, 1024:1536].astype(BF16)
        lane = lax.broadcasted_iota(jnp.int32, (1, LANES), 1)
        logf = jnp.where(lane < N_HEADS, _log_sigmoid(ub_ref[:, 2048:2176] + fb_ref[...]), 0.0)
        cum = _exact_dot(logf, _tri(tm, True), ones_first=True) + carry[...]
        for h in range(N_HEADS):
            cc_ref[h] = jnp.broadcast_to(cum[:, h:h + 1], (tm, LANES))
        cr_ref[...] = jnp.transpose(cum)[0:N_HEADS, :]
        carry[...] = cum[tm - 1:tm, :]

    blk = pl.BlockSpec((tm, D_HALF), lambda i: (i, 0))
    return pl.pallas_call(
        body, name="fox_prep", grid=(s // tm,),
        in_specs=[pl.BlockSpec((tm, SEC), lambda i: (i, 0)), pl.BlockSpec((1, LANES), lambda i: (0, 0)),
                  pl.BlockSpec((1, D_HALF), lambda i: (0, 0)), pl.BlockSpec((1, D_HALF), lambda i: (0, 0))],
        out_specs=[blk, blk, blk, pl.BlockSpec((N_HEADS, tm, LANES), lambda i: (0, i, 0)),
                   pl.BlockSpec((N_HEADS, tm), lambda i: (0, i))],
        out_shape=[jax.ShapeDtypeStruct((s, D_HALF), BF16)] * 3
                  + [jax.ShapeDtypeStruct((N_HEADS, s, LANES), F32), jax.ShapeDtypeStruct((N_HEADS, s), F32)],
        scratch_shapes=[pltpu.VMEM((1, LANES), F32)],
        compiler_params=_params("arbitrary"),
    )(u_b, fb, qg, kg)


ATT_T = 256


def _attn_fwd(q, k, v, cc, cr, u_b):
    s = q.shape[0]
    t = ATT_T
    nblk = s // t

    def body(q_ref, k_ref, v_ref, cc_ref, cr_ref, g_ref, o_ref, mix_ref, lse_ref, m_sc, l_sc, acc_sc):
        i = pl.program_id(0)
        j = pl.program_id(1)

        @pl.when(j == 0)
        def _():
            m_sc[...] = jnp.full_like(m_sc, NEG)
            l_sc[...] = jnp.zeros_like(l_sc)
            acc_sc[...] = jnp.zeros_like(acc_sc)

        def tile(on_diagonal):
            causal = _causal_tile(t) if on_diagonal else None
            left = lax.broadcasted_iota(jnp.int32, (1, LANES), 1) < HEAD
            for p in range(N_PAIRS):
                lanes = slice(p * LANES, (p + 1) * LANES)
                q2, k2, v2 = q_ref[:, lanes], k_ref[:, lanes], v_ref[:, lanes]
                acc2 = acc_sc[:, lanes]
                for e in range(2):
                    h = 2 * p + e
                    msk = left if e == 0 else jnp.logical_not(left)
                    sc = _dot_nt(jnp.where(msk, q2, jnp.zeros_like(q2)), k2)
                    sc = sc + (_wide(cc_ref[h]) - cr_ref[h:h + 1, :])
                    if on_diagonal:
                        sc = jnp.where(causal, sc, NEG)
                    m_prev = m_sc[h]
                    m_new = jnp.maximum(m_prev, jnp.max(sc, axis=1, keepdims=True))
                    alpha = jnp.exp(m_prev - m_new)
                    pm = jnp.exp(sc - _wide(m_new))
                    l_sc[h] = alpha * l_sc[h] + jnp.sum(pm, axis=1, keepdims=True)
                    m_sc[h] = m_new
                    pv = jnp.dot(pm.astype(BF16), v2, preferred_element_type=F32)
                    acc2 = jnp.where(msk, alpha * acc2 + pv, acc2)
                acc_sc[:, lanes] = acc2

        pl.when(j < i)(functools.partial(tile, False))
        pl.when(j == i)(functools.partial(tile, True))

        @pl.when(j == i)
        def _():
            left = lax.broadcasted_iota(jnp.int32, (1, LANES), 1) < HEAD
            for p in range(N_PAIRS):
                lanes = slice(p * LANES, (p + 1) * LANES)
                inv = jnp.where(left, 1.0 / l_sc[2 * p], 1.0 / l_sc[2 * p + 1])
                o = acc_sc[:, lanes] * inv
                o_ref[:, lanes] = o
                gate = g_ref[:, lanes]
                mix_ref[:, lanes] = o * (gate * _sigmoid(gate))
            for h in range(N_HEADS):
                lse_ref[h] = m_sc[h] + jnp.log(l_sc[h])

    qblk = pl.BlockSpec((t, D_HALF), lambda i, j: (i, 0))
    kblk = pl.BlockSpec((t, D_HALF), lambda i, j: (jnp.minimum(i, j), 0))
    return pl.pallas_call(
        body, name="fox_attn_fwd", grid=(nblk, nblk),
        in_specs=[qblk, kblk, kblk, pl.BlockSpec((N_HEADS, t, LANES), lambda i, j: (0, i, 0)),
                  pl.BlockSpec((N_HEADS, t), lambda i, j: (0, jnp.minimum(i, j))),
                  pl.BlockSpec((t, D_HALF), lambda i, j: (i, 3))],
        out_specs=[qblk, qblk, pl.BlockSpec((N_HEADS, t, LANES), lambda i, j: (0, i, 0))],
        out_shape=[jax.ShapeDtypeStruct((s, D_HALF), F32), jax.ShapeDtypeStruct((s, D_HALF), F32),
                   jax.ShapeDtypeStruct((N_HEADS, s, LANES), F32)],
        scratch_shapes=[pltpu.VMEM((N_HEADS, t, LANES), F32), pltpu.VMEM((N_HEADS, t, LANES), F32),
                        pltpu.VMEM((t, D_HALF), F32)],
        compiler_params=_params("parallel", "arbitrary"),
    )(q, k, v, cc, cr, u_b)


def _fox_post_bwd(dmix, o, u_b, tm=256):
    s = o.shape[0]

    def body(dm_ref, o_ref, g_ref, do_ref, dg_ref):
        gate = g_ref[...]
        sg = _sigmoid(gate)
        dm = dm_ref[...]
        do_ref[...] = (dm * (gate * sg)).astype(BF16)
        dg_ref[...] = dm * o_ref[...] * (sg * (1.0 + gate * (1.0 - sg)))

    blk = pl.BlockSpec((tm, D_HALF), lambda i: (i, 0))
    return pl.pallas_call(
        body, name="fox_post_bwd", grid=(s // tm,),
        in_specs=[blk, blk, pl.BlockSpec((tm, D_HALF), lambda i: (i, 3))], out_specs=[blk] * 2,
        out_shape=[jax.ShapeDtypeStruct((s, D_HALF), BF16), jax.ShapeDtypeStruct((s, D_HALF), F32)],
        compiler_params=_params("parallel"),
    )(dmix, o, u_b)


def _causal_tile(t):
    return lax.broadcasted_iota(jnp.int32, (t, t), 0) >= lax.broadcasted_iota(jnp.int32, (t, t), 1)


def _wide(x):
    return jnp.concatenate([x, x], axis=1)


def _attn_probs(q2, k2, v2, do2, msk, causal, bias, lse_rows):
    zero = jnp.zeros_like(q2)
    qh = jnp.where(msk, q2, zero)
    doh = jnp.where(msk, do2, zero)
    sc = _dot_nt(qh, k2) + bias
    if causal is not None:
        sc = jnp.where(causal, sc, NEG)
    pm = jnp.exp(sc - _wide(lse_rows))
    dp = _dot_nt(doh, v2)
    return qh, doh, pm, dp


def _attn_bwd_rowdot(q, k, v, do, lse, cc, cr):
    s = q.shape[0]
    t = ATT_T
    nblk = s // t

    def body(q_ref, k_ref, v_ref, do_ref, lse_ref, cc_ref, cr_ref, dd_ref, acc):
        i = pl.program_id(0)
        j = pl.program_id(1)

        @pl.when(j == 0)
        def _():
            acc[...] = jnp.zeros_like(acc)

        def tile(on_diagonal):
            causal = _causal_tile(t) if on_diagonal else None
            left = lax.broadcasted_iota(jnp.int32, (1, LANES), 1) < HEAD
            for p in range(N_PAIRS):
                lanes = slice(p * LANES, (p + 1) * LANES)
                q2, k2, v2, do2 = q_ref[:, lanes], k_ref[:, lanes], v_ref[:, lanes], do_ref[:, lanes]
                for e in range(2):
                    h = 2 * p + e
                    msk = left if e == 0 else jnp.logical_not(left)
                    bias = _wide(cc_ref[h]) - cr_ref[h:h + 1, :]
                    _, _, pm, dp = _attn_probs(q2, k2, v2, do2, msk, causal, bias, lse_ref[h])
                    acc[h] += jnp.sum(pm * dp, axis=1, keepdims=True)

        pl.when(j < i)(functools.partial(tile, False))
        pl.when(j == i)(functools.partial(tile, True))

        @pl.when(j == i)
        def _():
            dd_ref[...] = acc[...]

    qblk = pl.BlockSpec((t, D_HALF), lambda i, j: (i, 0))
    qcol = pl.BlockSpec((N_HEADS, t, LANES), lambda i, j: (0, i, 0))
    kblk = pl.BlockSpec((t, D_HALF), lambda i, j: (jnp.minimum(i, j), 0))
    return pl.pallas_call(
        body, name="fox_attn_rowdot", grid=(nblk, nblk),
        in_specs=[qblk, kblk, kblk, qblk, qcol, qcol, pl.BlockSpec((N_HEADS, t), lambda i, j: (0, jnp.minimum(i, j)))],
        out_specs=qcol, out_shape=jax.ShapeDtypeStruct((N_HEADS, s, LANES), F32),
        scratch_shapes=[pltpu.VMEM((N_HEADS, t, LANES), F32)],
        compiler_params=_params("parallel", "arbitrary"),
    )(q, k, v, do, lse, cc, cr)


def _attn_bwd(q, k, v, do, lse, dd, cc, cr):
    s = q.shape[0]
    t = ATT_T
    nblk = s // t

    def body(q_ref, k_ref, v_ref, do_ref, lse_ref, dd_ref, cc_ref, cr_ref,
             dq_ref, dk_ref, dv_ref, dcr_ref, dk_sc, dv_sc, dcr_sc):
        j = pl.program_id(0)
        i = pl.program_id(1)

        @pl.when(jnp.logical_and(j == 0, i == 0))
        def _():
            dq_ref[...] = jnp.zeros_like(dq_ref)

        @pl.when(i == 0)
        def _():
            dk_sc[...] = jnp.zeros_like(dk_sc)
            dv_sc[...] = jnp.zeros_like(dv_sc)
            dcr_sc[...] = jnp.zeros_like(dcr_sc)

        def tile(on_diagonal):
            causal = _causal_tile(t) if on_diagonal else None
            left = lax.broadcasted_iota(jnp.int32, (1, LANES), 1) < HEAD
            qrows = pl.ds(pl.multiple_of(i * t, t), t)
            for p in range(N_PAIRS):
                lanes = slice(p * LANES, (p + 1) * LANES)
                q2, k2, v2, do2 = q_ref[:, lanes], k_ref[:, lanes], v_ref[:, lanes], do_ref[:, lanes]
                zero = jnp.zeros_like(q2)
                dq2 = jnp.zeros((t, LANES), F32)
                dk2 = jnp.zeros((t, LANES), F32)
                dv2 = jnp.zeros((t, LANES), F32)
                for e in range(2):
                    h = 2 * p + e
                    msk = left if e == 0 else jnp.logical_not(left)
                    bias = _wide(cc_ref[h]) - cr_ref[h:h + 1, :]
                    qh, doh, pm, dp = _attn_probs(q2, k2, v2, do2, msk, causal, bias, lse_ref[h])
                    dsc = pm * (dp - _wide(dd_ref[h]))
                    dsb = dsc.astype(BF16)
                    dv2 += _dot_tn(pm.astype(BF16), doh)
                    dk2 += _dot_tn(dsb, qh)
                    dq2 += jnp.dot(dsb, jnp.where(msk, k2, zero), preferred_element_type=F32)
                    dcr_sc[h:h + 1, :] += -_colsum(dsc)
                dq_ref[qrows, lanes] += dq2 * ATT_SCALE
                dk_sc[:, lanes] += dk2
                dv_sc[:, lanes] += dv2

        pl.when(i > j)(functools.partial(tile, False))
        pl.when(i == j)(functools.partial(tile, True))

        @pl.when(i == nblk - 1)
        def _():
            dk_ref[...] = dk_sc[...]
            dv_ref[...] = dv_sc[...]
            dcr_ref[...] = dcr_sc[...]

    qblk = pl.BlockSpec((t, D_HALF), lambda j, i: (jnp.maximum(i, j), 0))
    qcol = pl.BlockSpec((N_HEADS, t, LANES), lambda j, i: (0, jnp.maximum(i, j), 0))
    kblk = pl.BlockSpec((t, D_HALF), lambda j, i: (j, 0))
    return pl.pallas_call(
        body, name="fox_attn_bwd", grid=(nblk, nblk),
        in_specs=[qblk, kblk, kblk, qblk, qcol, qcol, qcol, pl.BlockSpec((N_HEADS, t), lambda j, i: (0, j))],
        out_specs=[pl.BlockSpec((s, D_HALF), lambda j, i: (0, 0)), kblk, kblk,
                   pl.BlockSpec((N_HEADS, t), lambda j, i: (0, j))],
        out_shape=[jax.ShapeDtypeStruct((s, D_HALF), F32)] * 3 + [jax.ShapeDtypeStruct((N_HEADS, s), F32)],
        scratch_shapes=[pltpu.VMEM((t, D_HALF), F32), pltpu.VMEM((t, D_HALF), F32), pltpu.VMEM((N_HEADS, t), F32)],
        compiler_params=_params("arbitrary", "arbitrary"),
    )(q, k, v, do, lse, dd, cc, cr)


def _fox_prep_bwd(u_b, dq, dk, dv, dgate, dcum, fb, qg, kg, tm=256):
    s = u_b.shape[0]
    nb = s // tm

    def body(ub_ref, dq_ref, dk_ref, dv_ref, dg_ref, dc_ref, fb_ref, qg_ref, kg_ref,
             du_ref, dqg_ref, dkg_ref, dfb_ref, carry):
        i = pl.program_id(0)

        @pl.when(i == 0)
        def _():
            carry[...] = jnp.zeros_like(carry)
            dqg_ref[...] = jnp.zeros_like(dqg_ref)
            dkg_ref[...] = jnp.zeros_like(dkg_ref)
            dfb_ref[...] = jnp.zeros_like(dfb_ref)

        bd = _head_ones()
        for lo, g_ref, d_ref, dgain_ref in ((0, qg_ref, dq_ref, dqg_ref), (512, kg_ref, dk_ref, dkg_ref)):
            gain = g_ref[...]
            xh, rinv, _ = _head_rms(ub_ref[:, lo:lo + 512], gain, bd)
            dn = d_ref[...]
            dgain_ref[...] += _colsum(dn * xh)
            dxh = dn * gain
            du_ref[:, lo:lo + 512] = rinv * (dxh - xh * (_head_sum(dxh * xh, bd) * (1.0 / HEAD)))
        du_ref[:, 1024:1536] = dv_ref[...]
        du_ref[:, 1536:2048] = dg_ref[...]
        lane = lax.broadcasted_iota(jnp.int32, (1, LANES), 1)
        dc = dc_ref[...]
        dlogf = _exact_dot(dc, _tri(tm, False), ones_first=True) + carry[...]
        carry[...] += _colsum(dc)
        fl = ub_ref[:, 2048:2176] + fb_ref[...]
        dfl = jnp.where(lane < N_HEADS, dlogf * (1.0 - _sigmoid(fl)), 0.0)
        du_ref[:, 2048:2176] = dfl
        dfb_ref[...] += _colsum(dfl)

    rev = lambda w: pl.BlockSpec((tm, w), lambda i: (nb - 1 - i, 0))
    vec = lambda w: pl.BlockSpec((1, w), lambda i: (0, 0))
    return pl.pallas_call(
        body, name="fox_prep_bwd", grid=(nb,),
        in_specs=[rev(SEC)] + [rev(D_HALF)] * 4 + [rev(LANES), vec(LANES), vec(D_HALF), vec(D_HALF)],
        out_specs=[rev(SEC), vec(D_HALF), vec(D_HALF), vec(LANES)],
        out_shape=[jax.ShapeDtypeStruct((s, SEC), F32), jax.ShapeDtypeStruct((1, D_HALF), F32),
                   jax.ShapeDtypeStruct((1, D_HALF), F32), jax.ShapeDtypeStruct((1, LANES), F32)],
        scratch_shapes=[pltpu.VMEM((1, LANES), F32)],
        compiler_params=_params("arbitrary"),
    )(u_b, dq, dk, dv, dgate, dcum, fb, qg, kg)


def _merge(mix_a, mix_b, u_g, x, tgt, wa, wb, wo, fg, tm=256):
    s, d = x.shape

    def body(ma_ref, mb_ref, ug_ref, x_ref, t_ref, wa_ref, wb_ref, wo_ref, fg_ref,
             dx2_ref, dma_ref, dmb_ref, dug_ref, dwa_ref, dwb_ref, dwo_ref, dfg_ref, loss_ref):
        i = pl.program_id(0)

        @pl.when(i == 0)
        def _():
            for ref in (dwa_ref, dwb_ref, dwo_ref, dfg_ref, loss_ref):
                ref[...] = jnp.zeros_like(ref)

        wa_v, wb_v, wo_v, fg_v = wa_ref[...], wb_ref[...], wo_ref[...], fg_ref[...]
        ma = ma_ref[...].astype(BF16)
        mb = mb_ref[...].astype(BF16)
        ya = jnp.dot(ma, wa_v, preferred_element_type=F32)
        yb = jnp.dot(mb, wb_v, preferred_element_type=F32)
        sa = _sigmoid(ug_ref[:, 0:d])
        sb = _sigmoid(ug_ref[:, d:2 * d])
        merged = (sa * ya + sb * yb).astype(BF16)
        x2 = x_ref[...] + jnp.dot(merged, wo_v, preferred_element_type=F32)
        r2 = lax.rsqrt(jnp.mean(x2 * x2, axis=-1, keepdims=True) + RMS_EPS)
        x2h = x2 * r2
        err = x2h * fg_v - t_ref[...]
        loss_ref[...] += _colsum(err * err)
        dy = err * (1.0 / d)
        dfg_ref[...] += _colsum(dy * x2h)
        dx2h = dy * fg_v
        dx2 = r2 * (dx2h - x2h * jnp.mean(dx2h * x2h, axis=-1, keepdims=True))
        dx2_ref[...] = dx2
        dx2b = dx2.astype(BF16)
        dmerged = _dot_nt(dx2b, wo_v)
        dwo_ref[...] += _dot_tn(merged, dx2b)
        dya = dmerged * sa
        dyb = dmerged * sb
        dug_ref[:, 0:d] = dya * ya * (1.0 - sa)
        dug_ref[:, d:2 * d] = dyb * yb * (1.0 - sb)
        dyab = dya.astype(BF16)
        dybb = dyb.astype(BF16)
        dma_ref[...] = _dot_nt(dyab, wa_v)
        dmb_ref[...] = _dot_nt(dybb, wb_v)
        dwa_ref[...] += _dot_tn(ma, dyab)
        dwb_ref[...] += _dot_tn(mb, dybb)

    row = lambda w: pl.BlockSpec((tm, w), lambda i: (i, 0))
    full = lambda a: pl.BlockSpec(a.shape, lambda i: (0, 0))
    fshape = lambda a: jax.ShapeDtypeStruct(a.shape, F32)
    return pl.pallas_call(
        body, name="merge_fwd_bwd", grid=(s // tm,),
        in_specs=[row(D_HALF), row(D_HALF), row(GATE_COLS), row(d), row(d), full(wa), full(wb), full(wo), full(fg)],
        out_specs=[row(d), row(D_HALF), row(D_HALF), row(GATE_COLS), full(wa), full(wb), full(wo), full(fg), full(fg)],
        out_shape=[jax.ShapeDtypeStruct((s, d), F32), jax.ShapeDtypeStruct((s, D_HALF), F32),
                   jax.ShapeDtypeStruct((s, D_HALF), F32), jax.ShapeDtypeStruct((s, GATE_COLS), F32),
                   fshape(wa), fshape(wb), fshape(wo), fshape(fg), fshape(fg)],
        compiler_params=_params("arbitrary"),
    )(mix_a, mix_b, u_g, x, tgt, wa, wb, wo, fg)


def _lora_weight(w_up, a_up):
    z = jnp.zeros((LORA, D_HALF), w_up.dtype)
    return jnp.concatenate([jnp.concatenate([w_up, z], axis=1), jnp.concatenate([z, a_up], axis=1)], axis=0)


def _device_grads(x, tgt, p, w_a, w_up, a_up, late_weights, fwd_exchange=None, bwd_exchange=None, tail_exchange=None):
    wl = _lora_weight(w_up, a_up)
    rk = p["r_k"].reshape(1, D_HALF)
    fb = jnp.pad(p["f_bias"], ((0, 0), (0, LANES - N_HEADS)))
    qg = jnp.tile(p["q_norm_g"], (1, N_HEADS))
    kg = jnp.tile(p["k_norm_g"], (1, N_HEADS))
    fg = p["final_norm_g"].reshape(1, D_MODEL)
    mixer = (p["shift_mu"], wl, p["w0"], p["a0"], p["k_k"], p["k_a"])

    h = _rmsnorm_in(x, p["norm_g"])
    u_a = _matmul_nn(h, w_a, "inproj_rwkv")
    r, dec, k, v, av, bv, gate_a = _rwkv_prep(u_a, *mixer)
    y, st, arrived = _wkv_fwd(r, dec, k, av, bv, v, fwd_exchange)
    mix_a = _rwkv_post(y, r, k, v, gate_a, p["lnx_w"], p["lnx_b"], rk)

    w_b, w_g, w_out_a, w_out_b, w_out = late_weights(arrived)
    u_b = _matmul_nn(h, w_b, "inproj_fox")
    u_g = _matmul_nn(h, w_g, "inproj_gate")
    q, kn, vb, cc, cr = _fox_prep(u_b, fb, qg, kg)
    o, mix_b, lse = _attn_fwd(q, kn, vb, cc, cr, u_b)

    dx2, dmix_a, dmix_b, du_g, dwa, dwb, dwo, dfg, loss_vec = _merge(
        mix_a, mix_b, u_g, x, tgt, w_out_a, w_out_b, w_out, fg)

    do, dgate_b = _fox_post_bwd(dmix_b, o, u_b)
    dd = _attn_bwd_rowdot(q, kn, vb, do, lse, cc, cr)
    dq, dk_att, dv_att, dcr = _attn_bwd(q, kn, vb, do, lse, dd, cc, cr)
    dcum = jnp.pad(dcr.T, ((0, 0), (0, LANES - N_HEADS)))
    du_b, dqg, dkg, dfb = _fox_prep_bwd(u_b, dq, dk_att, dv_att, dgate_b, dcum, fb, qg, kg)
    h_t = h.T
    dw_b = _matmul_tn_acc(h_t, du_b, "dw_fox")
    dw_g = _matmul_tn_acc(h_t, du_g, "dw_gate")

    dy, dr_b, dk_b, dv_b, dgate_a, dlw, dlb, drk = _rwkv_post_bwd(
        dmix_a, y, r, k, v, gate_a, p["lnx_w"], p["lnx_b"], rk)
    scan_grads, sent = _wkv_bwd(r, dec, k, av, bv, v, dy, st,
                                bwd_exchange(dw_b, dw_g, dwa, dwb, dwo) if bwd_exchange else None)
    du_a, dmu, dwl, dw0, da0, dkkw, dkaw = _rwkv_prep_bwd(u_a, (*scan_grads, dr_b, dk_b, dv_b, dgate_a), *mixer)
    dw_a = _matmul_tn_acc(h_t, du_a, "dw_rwkv")
    dw_up, da_up = dwl[:LORA, :D_HALF], dwl[LORA:, D_HALF:]
    sent_last = _run_on_sequencer(tail_exchange(dw_a, dw_up, da_up), "scatter_tail", 1) if tail_exchange else []
    grad_x, dnorm_g, _ = _inproj_bwd(du_a, du_b, du_g, w_a, w_b, w_g, x, dx2, p["norm_g"])

    grads = dict(
        norm_g=dnorm_g, w_in=(dw_a, dw_b, dw_g), shift_mu=dmu,
        w_lora_up=dw_up, w0=dw0, a_lora_up=da_up, a0=da0, k_k=dkkw, k_a=dkaw,
        r_k=drk.reshape(1, N_HEADS, HEAD), lnx_w=dlw, lnx_b=dlb, f_bias=dfb[:, :N_HEADS],
        q_norm_g=dqg.reshape(N_HEADS, HEAD).sum(axis=0, keepdims=True),
        k_norm_g=dkg.reshape(N_HEADS, HEAD).sum(axis=0, keepdims=True),
        w_out_a=dwa, w_out_b=dwb, w_out=dwo, final_norm_g=dfg.reshape(D_MODEL))
    return loss_vec, grad_x, grads, sent, sent_last


CHIP_FLIPS = ((1, 0), (0, 1), (1, 1))
ANY = pl.BlockSpec(memory_space=pl.ANY)


def _position():
    return lax.axis_index("x"), lax.axis_index("y"), lax.axis_index("c")


def _flip(v, f):
    return 1 - v if f else v


def _both(a, b):
    if a is None:
        return b
    return a if b is None else jnp.logical_and(a, b)


def _when(cond, fn):
    if cond is None:
        fn()
    else:
        pl.when(cond)(fn)


class _Moves:
    def __init__(self, send_sems, recv_sems, local_sems):
        self.send_sems, self.recv_sems, self.local_sems = send_sems, recv_sems, local_sems
        self.remote, self.local = [], []

    def send(self, src, dst, peer, landing, send_if=None, recv_if=None):
        k = len(self.remote)
        sems = dict(send_sem=self.send_sems.at[k], recv_sem=self.recv_sems.at[k], device_id=peer, device_id_type=MESH)
        out = pltpu.make_async_remote_copy(src_ref=src, dst_ref=dst, **sems)
        arrival = pltpu.make_async_remote_copy(src_ref=src, dst_ref=landing, **sems)
        self.remote.append((out, arrival, send_if, recv_if))

    def copy(self, src, dst, cond=None):
        cp = pltpu.make_async_copy(src, dst, self.local_sems.at[len(self.local)])
        self.local.append((cp, cond))

    def start(self, also=None):
        for cp, cond in self.local:
            _when(_both(also, cond), cp.start)
        for out, _, send_if, _ in self.remote:
            _when(_both(also, send_if), out.start)

    def wait_arrivals(self, also=None):
        for _, arrival, _, recv_if in self.remote:
            _when(_both(also, recv_if), arrival.wait_recv)

    def wait_sent(self, also=None):
        for out, _, send_if, _ in self.remote:
            _when(_both(also, send_if), out.wait_send)
        for cp, cond in self.local:
            _when(_both(also, cond), cp.wait)

    def wait(self, also=None):
        self.wait_arrivals(also)
        self.wait_sent(also)


class _Exchange:
    def __init__(self, operands, out_shapes, n_remote, n_local, build, n_relay=0, relay=None):
        self.operands, self.out_shapes = list(operands), list(out_shapes)
        self.n_remote, self.n_local, self.build = n_remote, n_local, build
        self.n_relay, self.relay = n_relay, relay

    def scratch(self):
        return [pltpu.SemaphoreType.DMA((self.n_remote,)), pltpu.SemaphoreType.DMA((self.n_remote,)),
                pltpu.SemaphoreType.DMA((max(self.n_local, 1),))]

    def moves(self, in_refs, out_refs, sems):
        mv = _Moves(*sems)
        self.build(mv, in_refs, out_refs)
        return mv

    def run_alone(self, name):
        n_in, n_out = len(self.operands), len(self.out_shapes)
        relay_scratch = [pltpu.SemaphoreType.DMA((self.n_relay,))] * 2 if self.relay else []

        def body(*refs):
            ins, outs, sems = refs[:n_in], refs[n_in:n_in + n_out], refs[n_in + n_out:]
            mv = self.moves(ins, outs, sems[:3])
            mv.start()
            mv.wait_arrivals()
            if self.relay:
                passed = _Moves(sems[3], sems[4], None)
                self.relay(passed, ins, outs)
                passed.start()
                passed.wait()
            mv.wait_sent()

        return pl.pallas_call(
            body, name=name, in_specs=[ANY] * n_in, out_specs=[ANY] * n_out, out_shape=self.out_shapes,
            scratch_shapes=self.scratch() + relay_scratch, compiler_params=pltpu.CompilerParams(has_side_effects=True),
        )(*self.operands)


def _run_on_sequencer(exchange, name, collective_id):
    ins = [jax.new_ref(a, memory_space=pltpu.MemorySpace.HBM) for a in exchange.operands]
    outs = [jax.empty_ref(s, memory_space=pltpu.MemorySpace.HBM) for s in exchange.out_shapes]

    def launch(send_sems, recv_sems, local_sems):
        x, y, c = _position()
        barrier = pltpu.get_barrier_semaphore()
        for fx, fy in CHIP_FLIPS:
            pl.semaphore_signal(barrier, inc=1, device_id=(_flip(x, fx), _flip(y, fy), c), device_id_type=MESH)
        pl.semaphore_wait(barrier, len(CHIP_FLIPS))
        moves = exchange.moves(ins, outs, (send_sems, recv_sems, local_sems))
        moves.start()
        moves.wait()

    pl.kernel(launch, mesh=plsc.ScalarSubcoreMesh(axis_name="sequencer", num_cores=1), name=name,
              scratch_types=tuple(exchange.scratch()),
              compiler_params=pltpu.CompilerParams(collective_id=collective_id))()
    return [o[...] for o in outs]


def _is_chip(x, y, chip):
    return jnp.logical_and(x == chip // 2, y == chip % 2)


def _gather_exchange(from_chip, from_all, split=()):
    n1, n2 = len(from_chip), len(from_all)

    def rows_of(t, c):
        half = from_chip[t][1].shape[0] // 2
        return pl.ds(c * half, half)

    def build(mv, ins, outs):
        x, y, c = _position()
        me = 2 * x + y
        for t, (chip, _) in enumerate(from_chip):
            mv.copy(ins[t], outs[t], cond=_is_chip(x, y, chip))
        for t in range(n2):
            mv.copy(ins[n1 + t], outs[n1 + t].at[me])
        for fx, fy in CHIP_FLIPS:
            px, py = _flip(x, fx), _flip(y, fy)
            peer = (px, py, c)
            for t, (chip, _) in enumerate(from_chip):
                part = rows_of(t, c) if t in split else slice(None)
                mv.send(ins[t].at[part], outs[t].at[part], peer, landing=outs[t].at[part],
                        send_if=_is_chip(x, y, chip), recv_if=_is_chip(px, py, chip))
            for t in range(n2):
                mv.send(ins[n1 + t], outs[n1 + t].at[me], peer, landing=outs[n1 + t].at[2 * px + py])

    def relay(mv, ins, outs):
        x, y, c = _position()
        for t in split:
            came = jnp.logical_not(_is_chip(x, y, from_chip[t][0]))
            mv.send(outs[t].at[rows_of(t, c)], outs[t].at[rows_of(t, c)], (x, y, 1 - c),
                    landing=outs[t].at[rows_of(t, 1 - c)], send_if=came, recv_if=came)

    arrays = [a for _, a in from_chip] + list(from_all)
    shapes = [jax.ShapeDtypeStruct(a.shape, a.dtype) for _, a in from_chip]
    shapes += [jax.ShapeDtypeStruct((N_CHIPS,) + a.shape, a.dtype) for a in from_all]
    return _Exchange(arrays, shapes, len(CHIP_FLIPS) * (n1 + n2), n1 + n2, build,
                     n_relay=len(split), relay=relay if split else None)


def _scatter_exchange(to_chip, to_all):
    n1, n2 = len(to_chip), len(to_all)

    def build(mv, ins, outs):
        x, y, c = _position()
        for f, (fx, fy) in enumerate(CHIP_FLIPS):
            px, py = _flip(x, fx), _flip(y, fy)
            peer = (px, py, c)
            for t, (chip, _) in enumerate(to_chip):
                mv.send(ins[t], outs[t].at[f], peer, landing=outs[t].at[f],
                        send_if=_is_chip(px, py, chip), recv_if=_is_chip(x, y, chip))
            for t in range(n2):
                mv.send(ins[n1 + t].at[2 * px + py], outs[n1 + t].at[f], peer, landing=outs[n1 + t].at[f])

    arrays = [a for _, a in to_chip] + list(to_all)
    shapes = [jax.ShapeDtypeStruct((len(CHIP_FLIPS),) + a.shape, a.dtype) for _, a in to_chip]
    shapes += [jax.ShapeDtypeStruct((len(CHIP_FLIPS),) + a.shape[1:], a.dtype) for a in to_all]
    return _Exchange(arrays, shapes, len(CHIP_FLIPS) * (n1 + n2), 0, build)


def _swap_sibling(tensors):
    n = len(tensors)

    def body(*refs):
        ins, outs = refs[:n], refs[n:2 * n]
        send_sems, recv_sems = refs[2 * n:]
        x, y, c = _position()
        copies = [pltpu.make_async_remote_copy(
            src_ref=ins[t], dst_ref=outs[t], send_sem=send_sems.at[t], recv_sem=recv_sems.at[t],
            device_id=(x, y, 1 - c), device_id_type=MESH) for t in range(n)]
        for cp in copies:
            cp.start()
        for cp in copies:
            cp.wait_recv()
        for cp in copies:
            cp.wait_send()

    return pl.pallas_call(
        body, name="swap_sibling", in_specs=[ANY] * n, out_specs=[ANY] * n,
        out_shape=[jax.ShapeDtypeStruct(a.shape, a.dtype) for a in tensors],
        scratch_shapes=[pltpu.SemaphoreType.DMA((n,)), pltpu.SemaphoreType.DMA((n,))],
        compiler_params=pltpu.CompilerParams(has_side_effects=True),
    )(*tensors)


def _allreduce_small(slab):
    stages = 3

    def body(x_ref, o_ref, buf, send_sems, recv_sems):
        x, y, c = _position()
        peers = ((1 - x, y, c), (x, 1 - y, c), (x, y, 1 - c))
        o_ref[...] = x_ref[...]
        for k, peer in enumerate(peers):
            cp = pltpu.make_async_remote_copy(src_ref=o_ref, dst_ref=buf.at[k], send_sem=send_sems.at[k],
                                              recv_sem=recv_sems.at[k], device_id=peer, device_id_type=MESH)
            cp.start()
            cp.wait()
            o_ref[...] = o_ref[...] + buf[k]

    return pl.pallas_call(
        body, name="allreduce_small",
        in_specs=[pl.BlockSpec(memory_space=pltpu.VMEM)], out_specs=pl.BlockSpec(memory_space=pltpu.VMEM),
        out_shape=jax.ShapeDtypeStruct(slab.shape, slab.dtype),
        scratch_shapes=[pltpu.VMEM((stages,) + slab.shape, slab.dtype),
                        pltpu.SemaphoreType.DMA((stages,)), pltpu.SemaphoreType.DMA((stages,))],
        compiler_params=pltpu.CompilerParams(has_side_effects=True),
    )(slab)


def _row_tile(r):
    return min(r, 256)


def _sum4(stack, recv, me):
    _, r, c = stack.shape
    tr = _row_tile(r)

    def body(me_ref, own_ref, recv_ref, o_ref):
        o_ref[...] = (((own_ref[...] + recv_ref[0].astype(F32)) + recv_ref[1].astype(F32))
                      + recv_ref[2].astype(F32))

    return pl.pallas_call(
        body, name="sum_partials",
        grid_spec=pltpu.PrefetchScalarGridSpec(
            num_scalar_prefetch=1, grid=(r // tr,),
            in_specs=[pl.BlockSpec((None, tr, c), lambda i, me_ref: (me_ref[0], i, 0)),
                      pl.BlockSpec((len(CHIP_FLIPS), tr, c), lambda i, me_ref: (0, i, 0))],
            out_specs=pl.BlockSpec((tr, c), lambda i, me_ref: (i, 0))),
        out_shape=jax.ShapeDtypeStruct((r, c), F32), compiler_params=_params("parallel"),
    )(me, stack, recv)


def _sum_block(own, recv):
    r, c = own.shape
    tr = _row_tile(r)

    def body(own_ref, recv_ref, o_ref):
        o_ref[...] = (((own_ref[...] + recv_ref[0].astype(F32)) + recv_ref[1].astype(F32))
                      + recv_ref[2].astype(F32))

    return pl.pallas_call(
        body, name="sum_block", grid=(r // tr,),
        in_specs=[pl.BlockSpec((tr, c), lambda i: (i, 0)), pl.BlockSpec((len(CHIP_FLIPS), tr, c), lambda i: (0, i, 0))],
        out_specs=pl.BlockSpec((tr, c), lambda i: (i, 0)),
        out_shape=jax.ShapeDtypeStruct((r, c), F32), compiler_params=_params("parallel"),
    )(own, recv)


def _adamw_math(w, g, m, v):
    m = ADAM_B1 * m + (1.0 - ADAM_B1) * g
    v = ADAM_B2 * v + (1.0 - ADAM_B2) * (g * g)
    m_hat = m / (1.0 - ADAM_B1 ** ADAM_STEP)
    v_hat = v / (1.0 - ADAM_B2 ** ADAM_STEP)
    delta = -ADAM_LR * (m_hat / (jnp.sqrt(v_hat) + ADAM_EPS) + ADAM_WD * w)
    return delta, m, v


def _adamw(w, m, v, g_parts, name):
    r, c = w.shape
    tr = _row_tile(r)
    n = len(g_parts)

    def body(*refs):
        w_ref, m_ref, v_ref = refs[:3]
        g_refs = refs[3:3 + n]
        g_out, d_out, m_out, v_out = refs[3 + n:]
        g = g_refs[0][...]
        for ref in g_refs[1:]:
            g = g + ref[...]
        g_out[...] = g
        d_out[...], m_out[...], v_out[...] = _adamw_math(w_ref[...], g, m_ref[...], v_ref[...])

    blk = pl.BlockSpec((tr, c), lambda i: (i, 0))
    return pl.pallas_call(
        body, name=name, grid=(r // tr,), in_specs=[blk] * (3 + n), out_specs=[blk] * 4,
        out_shape=[jax.ShapeDtypeStruct((r, c), F32)] * 4, compiler_params=_params("parallel"),
    )(w, m, v, *g_parts)


SHARDED = ("w_in", "w_lora_up", "a_lora_up", "w_out_a", "w_out_b", "w_out")
ROW_SHARDED = ("w_out",)
SMALL = ("norm_g", "shift_mu", "w0", "a0", "k_k", "k_a", "r_k", "lnx_w", "lnx_b", "f_bias", "q_norm_g", "k_norm_g",
         "final_norm_g")
WEIGHTS = ("norm_g", "w_in", "shift_mu", "w_lora_up", "w0", "a_lora_up", "a0", "k_k", "k_a", "r_k", "lnx_w", "lnx_b",
           "f_bias", "q_norm_g", "k_norm_g", "w_out_a", "w_out_b", "w_out", "final_norm_g")
SLAB_ROWS = 16
SLAB_COLS = SEC


def _to_slab(named, extra=None):
    rows = [jnp.pad(named[n].reshape(1, -1), ((0, 0), (0, SLAB_COLS - named[n].size))) for n in SMALL]
    if extra is not None:
        rows.append(jnp.pad(extra.reshape(1, -1), ((0, 0), (0, SLAB_COLS - extra.size))))
    rows.append(jnp.zeros((SLAB_ROWS - len(rows), SLAB_COLS), F32))
    return jnp.concatenate(rows, axis=0)


def _from_slab(slab, shapes):
    return {n: slab[i, :math.prod(shapes[n])].reshape(shapes[n]) for i, n in enumerate(SMALL)}


def _by_chip(g, name):
    if name in ROW_SHARDED:
        return g.reshape(N_CHIPS, g.shape[0] // N_CHIPS, g.shape[1])
    r, c = g.shape
    return g.reshape(r, N_CHIPS, c // N_CHIPS).transpose(1, 0, 2)


def _from_chips(stack, name):
    if name in ROW_SHARDED:
        return stack.reshape(-1, stack.shape[2])
    _, r, c = stack.shape
    return stack.transpose(1, 0, 2).reshape(r, N_CHIPS * c)


def kernel(x, norm_g, w_in, shift_mu, w_lora_up, w0, a_lora_up, a0, k_k, k_a, r_k, lnx_w, lnx_b, f_bias, q_norm_g, k_norm_g, w_out_a, w_out_b, w_out, final_norm_g, loss_target, m_norm_g, m_w_in, m_shift_mu, m_w_lora_up, m_w0, m_a_lora_up, m_a0, m_k_k, m_k_a, m_r_k, m_lnx_w, m_lnx_b, m_f_bias, m_q_norm_g, m_k_norm_g, m_w_out_a, m_w_out_b, m_w_out, m_final_norm_g, v_norm_g, v_w_in, v_shift_mu, v_w_lora_up, v_w0, v_a_lora_up, v_a0, v_k_k, v_k_a, v_r_k, v_lnx_w, v_lnx_b, v_f_bias, v_q_norm_g, v_k_norm_g, v_w_out_a, v_w_out_b, v_w_out, v_final_norm_g):
    w = dict(norm_g=norm_g, w_in=w_in, shift_mu=shift_mu, w_lora_up=w_lora_up, w0=w0, a_lora_up=a_lora_up, a0=a0,
             k_k=k_k, k_a=k_a, r_k=r_k, lnx_w=lnx_w, lnx_b=lnx_b, f_bias=f_bias, q_norm_g=q_norm_g,
             k_norm_g=k_norm_g, w_out_a=w_out_a, w_out_b=w_out_b, w_out=w_out, final_norm_g=final_norm_g)
    m = dict(norm_g=m_norm_g, w_in=m_w_in, shift_mu=m_shift_mu, w_lora_up=m_w_lora_up, w0=m_w0,
             a_lora_up=m_a_lora_up, a0=m_a0, k_k=m_k_k, k_a=m_k_a, r_k=m_r_k, lnx_w=m_lnx_w, lnx_b=m_lnx_b,
             f_bias=m_f_bias, q_norm_g=m_q_norm_g, k_norm_g=m_k_norm_g, w_out_a=m_w_out_a, w_out_b=m_w_out_b,
             w_out=m_w_out, final_norm_g=m_final_norm_g)
    v = dict(norm_g=v_norm_g, w_in=v_w_in, shift_mu=v_shift_mu, w_lora_up=v_w_lora_up, w0=v_w0,
             a_lora_up=v_a_lora_up, a0=v_a0, k_k=v_k_k, k_a=v_k_a, r_k=v_r_k, lnx_w=v_lnx_w, lnx_b=v_lnx_b,
             f_bias=v_f_bias, q_norm_g=v_q_norm_g, k_norm_g=v_k_norm_g, w_out_a=v_w_out_a, w_out_b=v_w_out_b,
             w_out=v_w_out, final_norm_g=v_final_norm_g)
    shapes = {n: w[n].shape for n in WEIGHTS}

    shard = {n: w[n][0].astype(BF16) for n in SHARDED}
    late = ("w_out_a", "w_out_b", "w_out")
    loras = ("w_lora_up", "a_lora_up")
    w_in_head, w_in_tail = shard["w_in"][:, :A_TAIL], shard["w_in"][:, A_TAIL:]
    shard0, shard1_head, up_stack, aup_stack = _gather_exchange(
        [(0, shard["w_in"]), (1, w_in_head)], [shard[n] for n in loras], split=(0,)).run_alone("gather_early")
    w_a = jnp.concatenate([shard0, shard1_head], axis=1)

    def late_weights(arrived):
        shard1_tail, shard2, shard3 = arrived[:3]
        w_b = jnp.concatenate([shard1_tail, shard2[:, :B_TAIL], jnp.zeros((D_MODEL, SEC - FOX_REAL), BF16)], axis=1)
        w_g = jnp.concatenate([shard2[:, B_TAIL:], shard3], axis=1)
        return (w_b, w_g, *[_from_chips(s, n) for n, s in zip(late, arrived[3:])])

    own = {}

    def bwd_exchange(dw_b, dw_g, dwa, dwb, dwo):
        own["tail1"] = dw_b[:, :B_HEAD]
        own["block2"] = jnp.concatenate([dw_b[:, B_HEAD:FOX_REAL], dw_g[:, :G_HEAD]], axis=1)
        own["block3"] = dw_g[:, G_HEAD:]
        own.update({n: _by_chip(g, n) for n, g in zip(late, (dwa, dwb, dwo))})
        return _scatter_exchange([(1, own["tail1"].astype(BF16)), (2, own["block2"].astype(BF16)),
                                  (3, own["block3"].astype(BF16))], [own[n].astype(BF16) for n in late])

    def tail_exchange(dw_a, dw_up, da_up):
        own["block0"], own["head1"] = dw_a[:, :SHARD_COLS], dw_a[:, SHARD_COLS:]
        own.update({n: _by_chip(g, n) for n, g in zip(loras, (dw_up, da_up))})
        return _scatter_exchange([(0, own["block0"].astype(BF16)), (1, own["head1"].astype(BF16))],
                                 [own[n].astype(BF16) for n in loras])

    small = {n: w[n] for n in SMALL}
    loss_vec, grad_x, grads, sent, sent_last = _device_grads(
        x[0], loss_target[0], small, w_a, _from_chips(up_stack, "w_lora_up"), _from_chips(aup_stack, "a_lora_up"),
        late_weights, _gather_exchange([(1, w_in_tail), (2, shard["w_in"]), (3, shard["w_in"])], [shard[n] for n in late]),
        bwd_exchange, tail_exchange)

    total = _allreduce_small(_to_slab(grads, extra=loss_vec))
    loss = (0.5 / D_MODEL) * jnp.sum(total[len(SMALL)])
    slab_g, slab_d, slab_m, slab_v = _adamw(_to_slab(w), _to_slab(m), _to_slab(v), [total], "adamw_small")
    out_g, out_d, out_m, out_v = (_from_slab(s, shapes) for s in (total, slab_d, slab_m, slab_v))
    del slab_g

    xpos, ypos, _ = _position()
    me = (2 * xpos + ypos).astype(jnp.int32).reshape(1)
    core_sum = {"w_in": lax.switch(me[0], [
        lambda: _sum_block(own["block0"], sent_last[0]),
        lambda: jnp.concatenate([_sum_block(own["head1"], sent_last[1]), _sum_block(own["tail1"], sent[0])], axis=1),
        lambda: _sum_block(own["block2"], sent[1]),
        lambda: _sum_block(own["block3"], sent[2])])}
    core_sum.update({n: _sum4(own[n], r, me) for n, r in zip(late, sent[3:])})
    core_sum.update({n: _sum4(own[n], r, me) for n, r in zip(loras, sent_last[2:])})
    sibling_sums = _swap_sibling([core_sum[n] for n in SHARDED])
    for n, theirs in zip(SHARDED, sibling_sums):
        g, d, m2, v2 = _adamw(w[n][0], m[n][0], v[n][0], [core_sum[n], theirs], "adamw_" + n)
        out_g[n], out_d[n], out_m[n], out_v[n] = (a.reshape(shapes[n]) for a in (g, d, m2, v2))

    return (loss, grad_x.reshape(x.shape), *[out_g[n] for n in WEIGHTS], *[out_d[n] for n in WEIGHTS],
            *[out_m[n] for n in WEIGHTS], *[out_v[n] for n in WEIGHTS])
```

```python
import functools
import math

import jax
import jax.numpy as jnp
from jax import lax
from jax.experimental import pallas as pl
from jax.experimental.pallas import tpu as pltpu
from jax.experimental.pallas import tpu_sc as plsc

F32 = jnp.float32
BF16 = jnp.bfloat16

D_MODEL = 1024
D_HALF = 512
HEAD = 64
N_HEADS = 8
LORA = 64
RWKV_COLS = 2176
FOX_REAL = 2056
SEC = 2176
GATE_COLS = 2048
IN_COLS = 6280
N_CHIPS = 4
SHARD_COLS = IN_COLS // N_CHIPS
A_TAIL = RWKV_COLS - SHARD_COLS
B_HEAD = SHARD_COLS - A_TAIL
B_TAIL = FOX_REAL - B_HEAD
G_HEAD = SHARD_COLS - B_TAIL
RMS_EPS = 1e-6
LNX_EPS = 64e-5
ATT_SCALE = HEAD ** -0.5
NEG = -1e30

ADAM_LR = 0.001
ADAM_B1 = 0.9
ADAM_B2 = 0.999
ADAM_EPS = 1e-08
ADAM_WD = 0.01
ADAM_STEP = 10

LANES = 128
SUBLANES = 8
VMEM_LIMIT = 56 * 1024 * 1024
MESH = pl.DeviceIdType.MESH


def _params(*sem):
    return pltpu.CompilerParams(dimension_semantics=sem if sem else None, vmem_limit_bytes=VMEM_LIMIT)


def _sigmoid(x):
    return 1.0 / (1.0 + jnp.exp(-x))


def _log_sigmoid(x):
    return jnp.minimum(x, 0.0) - jnp.log(1.0 + jnp.exp(-jnp.abs(x)))


def _head_ones():
    r = lax.broadcasted_iota(jnp.int32, (LANES, LANES), 0) >> 6
    c = lax.broadcasted_iota(jnp.int32, (LANES, LANES), 1) >> 6
    return (r == c).astype(BF16)


def _split3(x):
    hi = x.astype(BF16)
    r1 = x - hi.astype(F32)
    mid = r1.astype(BF16)
    lo = (r1 - mid.astype(F32)).astype(BF16)
    return hi, mid, lo


def _exact_dot(x, ones_bf16, ones_first=False):
    out = None
    for piece in _split3(x):
        if ones_first:
            t = jnp.dot(ones_bf16, piece, preferred_element_type=F32)
        else:
            t = jnp.dot(piece, ones_bf16, preferred_element_type=F32)
        out = t if out is None else out + t
    return out


def _head_sum(x, bd):
    n = x.shape[1] // LANES
    parts = [_exact_dot(x[:, i * LANES:(i + 1) * LANES], bd) for i in range(n)]
    return parts[0] if n == 1 else jnp.concatenate(parts, axis=1)


def _dot_nt(a, b):
    return lax.dot_general(a, b, (((1,), (1,)), ((), ())), preferred_element_type=F32)


def _dot_tn(a, b):
    return lax.dot_general(a, b, (((0,), (0,)), ((), ())), preferred_element_type=F32)


def _colsum(x):
    return jnp.sum(x, axis=0, keepdims=True)


def _rmsnorm_in(x, g, tm=512):
    s, d = x.shape

    def body(x_ref, g_ref, h_ref):
        xv = x_ref[...]
        r = lax.rsqrt(jnp.mean(xv * xv, axis=-1, keepdims=True) + RMS_EPS)
        h_ref[...] = (xv * r * g_ref[...]).astype(BF16)

    return pl.pallas_call(
        body, name="rmsnorm_in", grid=(s // tm,),
        in_specs=[pl.BlockSpec((tm, d), lambda i: (i, 0)), pl.BlockSpec((1, d), lambda i: (0, 0))],
        out_specs=pl.BlockSpec((tm, d), lambda i: (i, 0)),
        out_shape=jax.ShapeDtypeStruct((s, d), BF16), compiler_params=_params("parallel"),
    )(x, g)


def _matmul_nn(a, b, name, tm=512):
    m, k = a.shape
    n = b.shape[1]

    def body(a_ref, b_ref, o_ref):
        o_ref[...] = jnp.dot(a_ref[...], b_ref[...], preferred_element_type=F32)

    return pl.pallas_call(
        body, name=name, grid=(m // tm,),
        in_specs=[pl.BlockSpec((tm, k), lambda i: (i, 0)), pl.BlockSpec((k, n), lambda i: (0, 0))],
        out_specs=pl.BlockSpec((tm, n), lambda i: (i, 0)),
        out_shape=jax.ShapeDtypeStruct((m, n), F32), compiler_params=_params("parallel"),
    )(a, b)


def _matmul_tn_acc(at, b, name, tk=512):
    m, k = at.shape
    n = b.shape[1]

    def body(a_ref, b_ref, o_ref):
        j = pl.program_id(0)

        @pl.when(j == 0)
        def _():
            o_ref[...] = jnp.zeros_like(o_ref)

        o_ref[...] += jnp.dot(a_ref[...], b_ref[...].astype(BF16), preferred_element_type=F32)

    return pl.pallas_call(
        body, name=name, grid=(k // tk,),
        in_specs=[pl.BlockSpec((m, tk), lambda j: (0, j)), pl.BlockSpec((tk, n), lambda j: (j, 0))],
        out_specs=pl.BlockSpec((m, n), lambda j: (0, 0)),
        out_shape=jax.ShapeDtypeStruct((m, n), F32), compiler_params=_params("arbitrary"),
    )(at, b)


def _inproj_bwd(du_a, du_b, du_g, w_a, w_b, w_g, x, dx2, g, exchange=None, tm=256):
    s, d = x.shape
    nb = s // tm

    def body(*refs):
        ((da_ref, db_ref, dg_ref, wa_ref, wb_ref, wg_ref, x_ref, dx2_ref, g_ref), (gx_ref, gg_ref), _,
         moves) = _split_refs(refs, 9, 2, exchange)
        i = pl.program_id(0)
        if moves:
            moves.start(also=(i == 0))

        @pl.when(i == 0)
        def _():
            gg_ref[...] = jnp.zeros_like(gg_ref)

        dh = _dot_nt(da_ref[...].astype(BF16), wa_ref[...])
        dh += _dot_nt(db_ref[...].astype(BF16), wb_ref[...])
        dh += _dot_nt(dg_ref[...].astype(BF16), wg_ref[...])
        xv = x_ref[...]
        r = lax.rsqrt(jnp.mean(xv * xv, axis=-1, keepdims=True) + RMS_EPS)
        xh = xv * r
        gg_ref[...] += _colsum(dh * xh)
        dxh = dh * g_ref[...]
        gx_ref[...] = dx2_ref[...] + r * (dxh - xh * jnp.mean(dxh * xh, axis=-1, keepdims=True))
        if moves:
            moves.wait(also=(i == nb - 1))

    row = lambda w: pl.BlockSpec((tm, w), lambda i: (i, 0))
    full = lambda a: pl.BlockSpec(a.shape, lambda i: (0, 0))
    ex_in = exchange.operands if exchange else []
    ex_out = exchange.out_shapes if exchange else []
    res = pl.pallas_call(
        body, name="inproj_bwd", grid=(nb,),
        in_specs=[row(SEC), row(SEC), row(GATE_COLS), full(w_a), full(w_b), full(w_g), row(d), row(d), full(g)]
                 + [ANY] * len(ex_in),
        out_specs=[row(d), pl.BlockSpec((1, d), lambda i: (0, 0))] + [ANY] * len(ex_out),
        out_shape=[jax.ShapeDtypeStruct((s, d), F32), jax.ShapeDtypeStruct((1, d), F32)] + ex_out,
        scratch_shapes=exchange.scratch() if exchange else [],
        compiler_params=_params("arbitrary"),
    )(du_a, du_b, du_g, w_a, w_b, w_g, x, dx2, g, *ex_in)
    return res[0], res[1], list(res[2:])


def _rwkv_elementwise(ua, prev_row, first, mu, wl, w0, a0, kkw, kaw, bd):
    tm = ua.shape[0]
    rows = lax.broadcasted_iota(jnp.int32, (tm, 1), 0)
    prev = jnp.where(first, jnp.zeros_like(prev_row), prev_row)
    shifted = jnp.where(rows == 0, prev, pltpu.roll(ua, 1, 0))
    delta = shifted - ua
    us = ua + delta * mu
    r = us[:, 0:512]
    k0 = us[:, 512:1024]
    v = us[:, 1024:1536]
    lo = us[:, 1536:1664]
    gate = us[:, 1664:2176]
    lane = lax.broadcasted_iota(jnp.int32, (1, LANES), 1)
    th = jnp.tanh(lo)
    lin = jnp.where(lane < LORA, th, lo)
    ll = jnp.dot(lin.astype(BF16), wl, preferred_element_type=F32)
    sz = _sigmoid(w0 + ll[:, :512])
    e = sz * math.exp(-0.5)
    dec = jnp.exp(-e)
    a = _sigmoid(a0 + ll[:, 512:])
    kk0 = k0 * kkw
    ss = _head_sum(kk0 * kk0, bd)
    nrm = jnp.maximum(jnp.sqrt(ss), 1e-12)
    kk = kk0 / nrm
    k = k0 * (1.0 + (a - 1.0) * kaw)
    return dict(delta=delta, us=us, r=r, k0=k0, v=v, lo=lo, gate=gate, th=th, lin=lin, sz=sz, e=e, dec=dec,
                a=a, kk0=kk0, ss=ss, nrm=nrm, kk=kk, k=k)


def _rwkv_prep(u_a, mu, wl, w0, a0, kkw, kaw, tm=256):
    s = u_a.shape[0]

    def body(ua_ref, prev_ref, mu_ref, wl_ref, w0_ref, a0_ref, kkw_ref, kaw_ref,
             r_ref, w_ref, k_ref, v_ref, a_ref, b_ref, g_ref):
        i = pl.program_id(0)
        f = _rwkv_elementwise(ua_ref[...], prev_ref[7:8, :], i == 0, mu_ref[...], wl_ref[...], w0_ref[...],
                              a0_ref[...], kkw_ref[...], kaw_ref[...], _head_ones())
        r_ref[...] = f["r"]
        w_ref[...] = f["dec"]
        k_ref[...] = f["k"]
        v_ref[...] = f["v"]
        a_ref[...] = -f["kk"]
        b_ref[...] = f["kk"] * f["a"]
        g_ref[...] = f["gate"]

    vec = lambda w: pl.BlockSpec((1, w), lambda i: (0, 0))
    out = pl.BlockSpec((tm, D_HALF), lambda i: (i, 0))
    return pl.pallas_call(
        body, name="rwkv_prep", grid=(s // tm,),
        in_specs=[pl.BlockSpec((tm, SEC), lambda i: (i, 0)),
                  pl.BlockSpec((8, SEC), lambda i: (jnp.maximum(i * (tm // 8) - 1, 0), 0)),
                  vec(SEC), pl.BlockSpec((LANES, 2 * D_HALF), lambda i: (0, 0)),
                  vec(D_HALF), vec(D_HALF), vec(D_HALF), vec(D_HALF)],
        out_specs=[out] * 7,
        out_shape=[jax.ShapeDtypeStruct((s, D_HALF), F32)] * 7,
        compiler_params=_params("parallel"),
    )(u_a, u_a, mu, wl, w0, a0, kkw, kaw)


SCAN_TB = 128
N_PAIRS = 4


def _pair_sum(x, left):
    s_l = jnp.sum(jnp.where(left, x, 0.0), axis=1, keepdims=True)
    s_r = jnp.sum(jnp.where(left, 0.0, x), axis=1, keepdims=True)
    return jnp.where(left, s_l, s_r)


def _pair_dot(x, row_l, row_r, left):
    s_l = jnp.sum(x * row_l, axis=1, keepdims=True)
    s_r = jnp.sum(x * row_r, axis=1, keepdims=True)
    return jnp.where(left, s_l, s_r)


def _halves(rows8):
    lane = lax.broadcasted_iota(jnp.int32, rows8.shape, 1)
    keep_left = (lane & (LANES - 1)) < HEAD
    return jnp.where(keep_left, rows8, 0.0), jnp.where(keep_left, 0.0, rows8)


def _quad_consts():
    lane = lax.broadcasted_iota(jnp.int32, (HEAD, 2 * LANES), 1)
    rowi = lax.broadcasted_iota(jnp.int32, (HEAD, 2 * LANES), 0)
    diag2 = rowi == (lane & (HEAD - 1))
    r = lax.broadcasted_iota(jnp.int32, (2 * LANES, 2 * LANES), 0) >> 6
    c = lax.broadcasted_iota(jnp.int32, (2 * LANES, 2 * LANES), 1) >> 6
    return diag2, (r == c).astype(BF16)


def _rows_to_columns(x8, diag2, bd2):
    lhs = jnp.concatenate([jnp.where(diag2, x8[i:i + 1], 0.0).astype(BF16) for i in range(SUBLANES)], axis=0)
    return jnp.dot(lhs, bd2, preferred_element_type=F32)


def _diag_rows(qtile, diag2, bd2, sub_row2):
    res = jnp.dot(qtile, bd2, preferred_element_type=F32)
    out = jnp.zeros((SUBLANES, 2 * LANES), F32)
    for i in range(SUBLANES):
        out = jnp.where(sub_row2 == i, _colsum(jnp.where(diag2, res[i * HEAD:(i + 1) * HEAD], 0.0)), out)
    return out


def _store_tile(qbuf, slot, p, i, x):
    qbuf[slot, p // 2, i * HEAD:(i + 1) * HEAD, (p % 2) * LANES:(p % 2 + 1) * LANES] = x.astype(BF16)


def _left_half():
    return lax.broadcasted_iota(jnp.int32, (HEAD, LANES), 1) < HEAD


def _split_refs(refs, n_rows, n_out, exchange):
    n_in = len(exchange.operands) if exchange else 0
    n_ex_out = len(exchange.out_shapes) if exchange else 0
    refs = list(refs)
    rows, refs = refs[:n_rows], refs[n_rows:]
    ex_in, refs = refs[:n_in], refs[n_in:]
    outs, refs = refs[:n_out], refs[n_out:]
    ex_out, refs = refs[:n_ex_out], refs[n_ex_out:]
    scratch, sems = (refs[:-3], refs[-3:]) if exchange else (refs, None)
    moves = exchange.moves(ex_in, ex_out, sems) if exchange else None
    return rows, outs, scratch, moves


def _wkv_fwd(r, w, k, a, b, v, exchange=None):
    s = r.shape[0]
    tb = SCAN_TB
    nb = s // tb

    def body(*refs):
        (r_ref, w_ref, k_ref, a_ref, b_ref, v_ref), (y_ref, st_ref), (state, vbuf, qbuf), moves = _split_refs(
            refs, 6, 2, exchange)
        g = pl.program_id(0)
        if moves:
            moves.start(also=(g == 0))

        @pl.when(g == 0)
        def _():
            state[...] = jnp.zeros_like(state)
            qbuf[...] = jnp.zeros_like(qbuf)

        left = _left_half()
        diag2, bd2 = _quad_consts()
        sub_row2 = lax.broadcasted_iota(jnp.int32, (SUBLANES, 2 * LANES), 0)
        groups = tb // SUBLANES
        quads = [slice(g2 * 2 * LANES, (g2 + 1) * 2 * LANES) for g2 in range(2)]

        def rows_of(q):
            return pl.ds(pl.multiple_of(q * SUBLANES, SUBLANES), SUBLANES)

        def v_tiles(q, slot):
            v8 = v_ref[rows_of(q), :]
            for g2 in range(2):
                vbuf[slot, g2] = _rows_to_columns(v8[:, quads[g2]], diag2, bd2)

        def chain(q, slot):
            rows8 = rows_of(q)
            a8, w8, b8, k8, r8 = (x[rows8, :] for x in (a_ref, w_ref, b_ref, k_ref, r_ref))
            pairs = [slice(p * LANES, (p + 1) * LANES) for p in range(N_PAIRS)]
            a_next = pltpu.roll(a8, SUBLANES - 1, 0)
            (a8_l, a8_r), (wa8_l, wa8_r) = _halves(a8), _halves(w8 * a_next)
            ba8 =jnp.concatenate([_pair_sum(b8[:, pr] * a_next[:, pr], left[0:SUBLANES]) for pr in pairs], axis=1)
            ka8 = jnp.concatenate([_pair_sum(k8[:, pr] * a_next[:, pr], left[0:SUBLANES]) for pr in pairs], axis=1)
            sp = [state[p] for p in range(N_PAIRS)]
            for i in range(0, SUBLANES, 2):
                r0, r1 = slice(i, i + 1), slice(i + 1, i + 2)
                sums = [(_pair_dot(sp[p], a8_l[r0, pairs[p]], a8_r[r0, pairs[p]], left),
                         _pair_dot(sp[p], wa8_l[r0, pairs[p]], wa8_r[r0, pairs[p]], left)) for p in range(N_PAIRS)]
                sa0, sa1 = [s[0] for s in sums], [s[1] for s in sums]
                for p in range(N_PAIRS):
                    pr = pairs[p]
                    inner = slice((p % 2) * LANES, (p % 2 + 1) * LANES)
                    vt0 = vbuf[slot, p // 2, i * HEAD:(i + 1) * HEAD, inner]
                    vt1 = vbuf[slot, p // 2, (i + 1) * HEAD:(i + 2) * HEAD, inner]
                    sa_next = sa1[p] + sa0[p] * ba8[r0, pr] + vt0 * ka8[r0, pr]
                    s1 = sp[p] * w8[r0, pr] + sa0[p] * b8[r0, pr] + vt0 * k8[r0, pr]
                    st_ref[q * SUBLANES + i, p] = s1
                    _store_tile(qbuf, slot, p, i, s1 * r8[r0, pr])
                    s2 = s1 * w8[r1, pr] + sa_next * b8[r1, pr] + vt1 * k8[r1, pr]
                    st_ref[q * SUBLANES + i + 1, p] = s2
                    _store_tile(qbuf, slot, p, i + 1, s2 * r8[r1, pr])
                    sp[p] = s2
            for p in range(N_PAIRS):
                state[p] = sp[p]

        def y_rows(q, slot):
            for g2 in range(2):
                y_ref[rows_of(q), quads[g2]] = _diag_rows(qbuf[slot, g2], diag2, bd2, sub_row2)

        v_tiles(0, 0)

        def two_groups(j, carry):
            q0 = 2 * j
            v_tiles(q0 + 1, 1)
            chain(q0, 0)
            y_rows(jnp.maximum(q0 - 1, 0), 1)
            v_tiles(jnp.minimum(q0 + 2, groups - 1), 0)
            chain(q0 + 1, 1)
            y_rows(q0, 0)
            return carry

        lax.fori_loop(0, groups // 2, two_groups, 0)
        y_rows(groups - 1, 1)
        if moves:
            moves.wait(also=(g == nb - 1))

    rows = pl.BlockSpec((tb, D_HALF), lambda g: (g, 0))
    ex_in = exchange.operands if exchange else []
    ex_out = exchange.out_shapes if exchange else []
    res = pl.pallas_call(
        body, name="wkv_fwd", grid=(nb,),
        in_specs=[rows] * 6 + [ANY] * len(ex_in),
        out_specs=[rows, pl.BlockSpec((tb, N_PAIRS, HEAD, LANES), lambda g: (g, 0, 0, 0))] + [ANY] * len(ex_out),
        out_shape=[jax.ShapeDtypeStruct((s, D_HALF), F32),
                   jax.ShapeDtypeStruct((s, N_PAIRS, HEAD, LANES), F32)] + ex_out,
        scratch_shapes=[pltpu.VMEM((N_PAIRS, HEAD, LANES), F32),
                        pltpu.VMEM((2, 2, SUBLANES * HEAD, 2 * LANES), F32),
                        pltpu.VMEM((2, 2, SUBLANES * HEAD, 2 * LANES), BF16)]
                       + (exchange.scratch() if exchange else []),
        compiler_params=_params("arbitrary"),
    )(r, w, k, a, b, v, *ex_in)
    return res[0], res[1], list(res[2:])


def _wkv_bwd(r, w, k, a, b, v, dy, st, exchange=None):
    s = r.shape[0]
    tb = SCAN_TB
    nb = s // tb

    def body(*refs):
        ((r_ref, w_ref, k_ref, a_ref, b_ref, v_ref, dy_ref, st_ref, before_ref),
         (dr_ref, dw_ref, dk_ref, dv_ref, da_ref, db_ref), (dstate, vbuf, qbuf, sbuf),
         moves) = _split_refs(refs, 9, 6, exchange)
        g = pl.program_id(0)
        first_block = g == nb - 1
        if moves:
            moves.start(also=(g == 0))

        @pl.when(g == 0)
        def _():
            dstate[...] = jnp.zeros_like(dstate)
            qbuf[...] = jnp.zeros_like(qbuf)

        left = _left_half()
        diag2, bd2 = _quad_consts()
        sub_row = lax.broadcasted_iota(jnp.int32, (SUBLANES, LANES), 0)
        sub_row2 = lax.broadcasted_iota(jnp.int32, (SUBLANES, 2 * LANES), 0)
        groups = tb // SUBLANES
        quads = [slice(g2 * 2 * LANES, (g2 + 1) * 2 * LANES) for g2 in range(2)]
        row_refs = (dr_ref, dw_ref, dk_ref, da_ref, db_ref)

        def rows_of(q):
            return pl.ds(pl.multiple_of(q * SUBLANES, SUBLANES), SUBLANES)

        def state_before(q, i, p):
            if i > 0:
                return st_ref[q * SUBLANES + i - 1, p]
            return jnp.where(q == 0, jnp.where(first_block, 0.0, before_ref[0, p]),
                             st_ref[jnp.maximum(q * SUBLANES - 1, 0), p])

        def column_tiles(q, slot):
            rows8 = rows_of(q)
            for kind, ref in enumerate((v_ref, dy_ref)):
                x8 = ref[rows8, :]
                for g2 in range(2):
                    vbuf[slot, kind, g2] = _rows_to_columns(x8[:, quads[g2]], diag2, bd2)
            a8 = a_ref[rows8, :]
            for i in range(SUBLANES):
                for p in range(N_PAIRS):
                    _store_tile(sbuf, 0, p, i, state_before(q, i, p) * a8[i:i + 1, p * LANES:(p + 1) * LANES])
            for g2 in range(2):
                vbuf[slot, 2, g2] = jnp.dot(sbuf[0, g2], bd2, preferred_element_type=F32)

        def chain(q, slot):
            rows8 = rows_of(q)
            a8, w8, b8, k8, r8 = (x[rows8, :] for x in (a_ref, w_ref, b_ref, k_ref, r_ref))
            b8_l, b8_r = _halves(b8)
            dsp = [dstate[p] for p in range(N_PAIRS)]
            outs = [[jnp.zeros((SUBLANES, LANES), F32) for _ in row_refs] for _ in range(N_PAIRS)]
            after = [st_ref[q * SUBLANES + SUBLANES - 1, p] for p in range(N_PAIRS)]
            for i in reversed(range(SUBLANES)):
                row = slice(i, i + 1)
                pl_ = [slice(p * LANES, (p + 1) * LANES) for p in range(N_PAIRS)]
                tile = [(p // 2, slice(i * HEAD, (i + 1) * HEAD), slice((p % 2) * LANES, (p % 2 + 1) * LANES))
                        for p in range(N_PAIRS)]
                sp = [state_before(q, i, p) for p in range(N_PAIRS)]
                dyt = [vbuf[(slot, 1) + tile[p]] for p in range(N_PAIRS)]
                ds = [dsp[p] + dyt[p] * r8[row, pl_[p]] for p in range(N_PAIRS)]
                dsa = [_pair_dot(ds[p], b8_l[row, pl_[p]], b8_r[row, pl_[p]], left) for p in range(N_PAIRS)]
                sa = [vbuf[(slot, 2) + tile[p]] for p in range(N_PAIRS)]
                for p in range(N_PAIRS):
                    ar, wr, br, kr = (x[row, pl_[p]] for x in (a8, w8, b8, k8))
                    vt = vbuf[(slot, 0) + tile[p]]
                    dsp[p] = ds[p] * wr + dsa[p] * ar
                    new = (_colsum(after[p] * dyt[p]), _colsum(ds[p] * sp[p]), _colsum(ds[p] * vt),
                           _colsum(sp[p] * dsa[p]), _colsum(ds[p] * sa[p]))
                    outs[p] = [jnp.where(sub_row == i, n, o) for n, o in zip(new, outs[p])]
                    _store_tile(qbuf, slot, p, i, ds[p] * kr)
                after = sp
            for p in range(N_PAIRS):
                dstate[p] = dsp[p]
                for ref, o in zip(row_refs, outs[p]):
                    ref[rows8, p * LANES:(p + 1) * LANES] = o

        def dv_rows(q, slot):
            for g2 in range(2):
                dv_ref[rows_of(q), quads[g2]] = _diag_rows(qbuf[slot, g2], diag2, bd2, sub_row2)

        column_tiles(groups - 1, 0)

        def two_groups(j, carry):
            q0 = groups - 1 - 2 * j
            column_tiles(q0 - 1, 1)
            chain(q0, 0)
            dv_rows(jnp.minimum(q0 + 1, groups - 1), 1)
            column_tiles(jnp.maximum(q0 - 2, 0), 0)
            chain(q0 - 1, 1)
            dv_rows(q0, 0)
            return carry

        lax.fori_loop(0, groups // 2, two_groups, 0)
        dv_rows(0, 1)
        if moves:
            moves.wait(also=(g == nb - 1))

    rows = pl.BlockSpec((tb, D_HALF), lambda g: (nb - 1 - g, 0))
    ex_in = exchange.operands if exchange else []
    ex_out = exchange.out_shapes if exchange else []
    res = pl.pallas_call(
        body, name="wkv_bwd", grid=(nb,),
        in_specs=[rows] * 7 + [pl.BlockSpec((tb, N_PAIRS, HEAD, LANES), lambda g: (nb - 1 - g, 0, 0, 0)),
                               pl.BlockSpec((1, N_PAIRS, HEAD, LANES),
                                            lambda g: (jnp.maximum((nb - 1 - g) * tb - 1, 0), 0, 0, 0))]
                 + [ANY] * len(ex_in),
        out_specs=[rows] * 6 + [ANY] * len(ex_out),
        out_shape=[jax.ShapeDtypeStruct((s, D_HALF), F32)] * 6 + ex_out,
        scratch_shapes=[pltpu.VMEM((N_PAIRS, HEAD, LANES), F32),
                        pltpu.VMEM((2, 3, 2, SUBLANES * HEAD, 2 * LANES), F32),
                        pltpu.VMEM((2, 2, SUBLANES * HEAD, 2 * LANES), BF16),
                        pltpu.VMEM((1, 2, SUBLANES * HEAD, 2 * LANES), BF16)]
                       + (exchange.scratch() if exchange else []),
        compiler_params=_params("arbitrary"),
    )(r, w, k, a, b, v, dy, st, st, *ex_in)
    return list(res[:6]), list(res[6:])


def _rwkv_post_math(y, r, k, v, gate, lw, lb, rk, bd):
    mean = _head_sum(y, bd) * (1.0 / HEAD)
    yc = y - mean
    var = _head_sum(yc * yc, bd) * (1.0 / HEAD)
    rstd = lax.rsqrt(var + LNX_EPS)
    yn = yc * rstd
    rkk = _head_sum(r * k * rk, bd)
    sg = _sigmoid(gate)
    pre = yn * lw + lb + rkk * v
    return yn, rstd, rkk, sg, pre


def _rwkv_post(y, r, k, v, gate, lw, lb, rk, tm=256):
    s = y.shape[0]

    def body(y_ref, r_ref, k_ref, v_ref, g_ref, lw_ref, lb_ref, rk_ref, o_ref):
        gate_v = g_ref[...]
        _, _, _, sg, pre = _rwkv_post_math(y_ref[...], r_ref[...], k_ref[...], v_ref[...], gate_v,
                                           lw_ref[...], lb_ref[...], rk_ref[...], _head_ones())
        o_ref[...] = pre * (gate_v * sg)

    blk = pl.BlockSpec((tm, D_HALF), lambda i: (i, 0))
    vec = pl.BlockSpec((1, D_HALF), lambda i: (0, 0))
    return pl.pallas_call(
        body, name="rwkv_post", grid=(s // tm,),
        in_specs=[blk] * 5 + [vec] * 3, out_specs=blk,
        out_shape=jax.ShapeDtypeStruct((s, D_HALF), F32), compiler_params=_params("parallel"),
    )(y, r, k, v, gate, lw, lb, rk)


def _rwkv_post_bwd(dmix, y, r, k, v, gate, lw, lb, rk, tm=256):
    s = y.shape[0]

    def body(dm_ref, y_ref, r_ref, k_ref, v_ref, g_ref, lw_ref, lb_ref, rk_ref,
             dy_ref, dr_ref, dk_ref, dv_ref, dg_ref, dlw_ref, dlb_ref, drk_ref):
        i = pl.program_id(0)

        @pl.when(i == 0)
        def _():
            dlw_ref[...] = jnp.zeros_like(dlw_ref)
            dlb_ref[...] = jnp.zeros_like(dlb_ref)
            drk_ref[...] = jnp.zeros_like(drk_ref)

        bd = _head_ones()
        rv, kv, vv, gate_v, lw_v, rk_v = r_ref[...], k_ref[...], v_ref[...], g_ref[...], lw_ref[...], rk_ref[...]
        yn, rstd, rkk, sg, pre = _rwkv_post_math(y_ref[...], rv, kv, vv, gate_v, lw_v, lb_ref[...], rk_v, bd)
        dm = dm_ref[...]
        dg_ref[...] = dm * pre * (sg * (1.0 + gate_v * (1.0 - sg)))
        dpre = dm * (gate_v * sg)
        dlw_ref[...] += _colsum(dpre * yn)
        dlb_ref[...] += _colsum(dpre)
        dyn = dpre * lw_v
        m1 = _head_sum(dyn, bd) * (1.0 / HEAD)
        m2 = _head_sum(dyn * yn, bd) * (1.0 / HEAD)
        dy_ref[...] = rstd * (dyn - m1 - yn * m2)
        dv_ref[...] = dpre * rkk
        drkk = _head_sum(dpre * vv, bd)
        dr_ref[...] = drkk * kv * rk_v
        dk_ref[...] = drkk * rv * rk_v
        drk_ref[...] += _colsum(drkk * rv * kv)

    blk = pl.BlockSpec((tm, D_HALF), lambda i: (i, 0))
    vec = pl.BlockSpec((1, D_HALF), lambda i: (0, 0))
    return pl.pallas_call(
        body, name="rwkv_post_bwd", grid=(s // tm,),
        in_specs=[blk] * 6 + [vec] * 3, out_specs=[blk] * 5 + [vec] * 3,
        out_shape=[jax.ShapeDtypeStruct((s, D_HALF), F32)] * 5 + [jax.ShapeDtypeStruct((1, D_HALF), F32)] * 3,
        compiler_params=_params("arbitrary"),
    )(dmix, y, r, k, v, gate, lw, lb, rk)


def _rwkv_prep_bwd(u_a, grads, mu, wl, w0, a0, kkw, kaw, tm=256):
    s = u_a.shape[0]
    nb = s // tm

    def body(ua_ref, prev_ref, drs_ref, dws_ref, dks_ref, dvs_ref, das_ref, dbs_ref, drb_ref, dkb_ref, dvb_ref,
             dgt_ref, mu_ref, wl_ref, w0_ref, a0_ref, kkw_ref, kaw_ref,
             du_ref, dmu_ref, dwl_ref, dw0_ref, da0_ref, dkkw_ref, dkaw_ref, carry):
        i = pl.program_id(0)

        @pl.when(i == 0)
        def _():
            carry[...] = jnp.zeros_like(carry)
            for ref in (dmu_ref, dwl_ref, dw0_ref, da0_ref, dkkw_ref, dkaw_ref):
                ref[...] = jnp.zeros_like(ref)

        bd = _head_ones()
        mu_v, wl_v, kkw_v, kaw_v = mu_ref[...], wl_ref[...], kkw_ref[...], kaw_ref[...]
        f = _rwkv_elementwise(ua_ref[...], prev_ref[7:8, :], i == nb - 1, mu_v, wl_v, w0_ref[...],
                              a0_ref[...], kkw_v, kaw_v, bd)
        a, kk, k0 = f["a"], f["kk"], f["k0"]
        dk = dks_ref[...] + dkb_ref[...]
        dbs = dbs_ref[...]
        dkk = dbs * a - das_ref[...]
        da = dbs * kk + dk * k0 * kaw_v
        dk0 = dk * (1.0 + (a - 1.0) * kaw_v)
        dkaw_ref[...] += _colsum(dk * k0 * (a - 1.0))
        inv = 1.0 / f["nrm"]
        proj = _head_sum(dkk * kk, bd)
        dkk0 = jnp.where(f["ss"] > 1e-24, (dkk - kk * proj) * inv, dkk * inv)
        dk0 = dk0 + dkk0 * kkw_v
        dkkw_ref[...] += _colsum(dkk0 * k0)
        dza = da * a * (1.0 - a)
        da0_ref[...] += _colsum(dza)
        dz = -dws_ref[...] * f["dec"] * f["e"] * (1.0 - f["sz"])
        dw0_ref[...] += _colsum(dz)
        dll = jnp.concatenate([dz, dza], axis=1).astype(BF16)
        dwl_ref[...] += _dot_tn(f["lin"].astype(BF16), dll)
        dlin = _dot_nt(dll, wl_v)
        lane = lax.broadcasted_iota(jnp.int32, (1, LANES), 1)
        th = f["th"]
        dlo = jnp.where(lane < LORA, dlin * (1.0 - th * th), dlin)
        dus = jnp.concatenate([drs_ref[...] + drb_ref[...], dk0, dvs_ref[...] + dvb_ref[...], dlo, dgt_ref[...]],
                              axis=1)
        dmu_ref[...] += _colsum(dus * f["delta"])
        g1 = dus * mu_v
        rows = lax.broadcasted_iota(jnp.int32, (tm, 1), 0)
        up = jnp.where(rows == tm - 1, carry[...], pltpu.roll(g1, tm - 1, 0))
        du_ref[...] = dus - g1 + up
        carry[...] = g1[0:1, :]

    rev = lambda w: pl.BlockSpec((tm, w), lambda i: (nb - 1 - i, 0))
    vec = lambda w: pl.BlockSpec((1, w), lambda i: (0, 0))
    wl_spec = pl.BlockSpec((LANES, 2 * D_HALF), lambda i: (0, 0))
    return pl.pallas_call(
        body, name="rwkv_prep_bwd", grid=(nb,),
        in_specs=[rev(SEC), pl.BlockSpec((8, SEC), lambda i: (jnp.maximum((nb - 1 - i) * (tm // 8) - 1, 0), 0))]
                 + [rev(D_HALF)] * 10 + [vec(SEC), wl_spec] + [vec(D_HALF)] * 4,
        out_specs=[rev(SEC), vec(SEC), wl_spec] + [vec(D_HALF)] * 4,
        out_shape=[jax.ShapeDtypeStruct((s, SEC), F32), jax.ShapeDtypeStruct((1, SEC), F32),
                   jax.ShapeDtypeStruct((LANES, 2 * D_HALF), F32)] + [jax.ShapeDtypeStruct((1, D_HALF), F32)] * 4,
        scratch_shapes=[pltpu.VMEM((1, SEC), F32)],
        compiler_params=_params("arbitrary"),
    )(u_a, u_a, *grads, mu, wl, w0, a0, kkw, kaw)


def _tri(tm, lower):
    r = lax.broadcasted_iota(jnp.int32, (tm, tm), 0)
    c = lax.broadcasted_iota(jnp.int32, (tm, tm), 1)
    return ((r >= c) if lower else (r <= c)).astype(BF16)


def _head_rms(x, g, bd):
    rinv = lax.rsqrt(_head_sum(x * x, bd) * (1.0 / HEAD) + RMS_EPS)
    xh = x * rinv
    return xh, rinv, xh * g


def _fox_prep(u_b, fb, qg, kg, tm=256):
    s = u_b.shape[0]

    def body(ub_ref, fb_ref, qg_ref, kg_ref, q_ref, k_ref, v_ref, cc_ref, cr_ref, carry):
        i = pl.program_id(0)

        @pl.when(i == 0)
        def _():
            carry[...] = jnp.zeros_like(carry)

        bd = _head_ones()
        _, _, qn = _head_rms(ub_ref[:, 0:512], qg_ref[...], bd)
        _, _, kn = _head_rms(ub_ref[:, 512:1024], kg_ref[...], bd)
        q_ref[...] = (qn * ATT_SCALE).astype(BF16)
        k_ref[...] = kn.astype(BF16)
        v_ref[...] = ub_ref[:, 1024:1536].astype(BF16)
        lane = lax.broadcasted_iota(jnp.int32, (1, LANES), 1)
        logf = jnp.where(lane < N_HEADS, _log_sigmoid(ub_ref[:, 2048:2176] + fb_ref[...]), 0.0)
        cum = _exact_dot(logf, _tri(tm, True), ones_first=True) + carry[...]
        for h in range(N_HEADS):
            cc_ref[h] = jnp.broadcast_to(cum[:, h:h + 1], (tm, LANES))
        cr_ref[...] = jnp.transpose(cum)[0:N_HEADS, :]
        carry[...] = cum[tm - 1:tm, :]

    blk = pl.BlockSpec((tm, D_HALF), lambda i: (i, 0))
    return pl.pallas_call(
        body, name="fox_prep", grid=(s // tm,),
        in_specs=[pl.BlockSpec((tm, SEC), lambda i: (i, 0)), pl.BlockSpec((1, LANES), lambda i: (0, 0)),
                  pl.BlockSpec((1, D_HALF), lambda i: (0, 0)), pl.BlockSpec((1, D_HALF), lambda i: (0, 0))],
        out_specs=[blk, blk, blk, pl.BlockSpec((N_HEADS, tm, LANES), lambda i: (0, i, 0)),
                   pl.BlockSpec((N_HEADS, tm), lambda i: (0, i))],
        out_shape=[jax.ShapeDtypeStruct((s, D_HALF), BF16)] * 3
                  + [jax.ShapeDtypeStruct((N_HEADS, s, LANES), F32), jax.ShapeDtypeStruct((N_HEADS, s), F32)],
        scratch_shapes=[pltpu.VMEM((1, LANES), F32)],
        compiler_params=_params("arbitrary"),
    )(u_b, fb, qg, kg)


ATT_T = 256


def _attn_fwd(q, k, v, cc, cr, u_b):
    s = q.shape[0]
    t = ATT_T
    nblk = s // t

    def body(q_ref, k_ref, v_ref, cc_ref, cr_ref, g_ref, o_ref, mix_ref, lse_ref, m_sc, l_sc, acc_sc):
        i = pl.program_id(0)
        j = pl.program_id(1)

        @pl.when(j == 0)
        def _():
            m_sc[...] = jnp.full_like(m_sc, NEG)
            l_sc[...] = jnp.zeros_like(l_sc)
            acc_sc[...] = jnp.zeros_like(acc_sc)

        def tile(on_diagonal):
            causal = _causal_tile(t) if on_diagonal else None
            left = lax.broadcasted_iota(jnp.int32, (1, LANES), 1) < HEAD
            for p in range(N_PAIRS):
                lanes = slice(p * LANES, (p + 1) * LANES)
                q2, k2, v2 = q_ref[:, lanes], k_ref[:, lanes], v_ref[:, lanes]
                acc2 = acc_sc[:, lanes]
                for e in range(2):
                    h = 2 * p + e
                    msk = left if e == 0 else jnp.logical_not(left)
                    sc = _dot_nt(jnp.where(msk, q2, jnp.zeros_like(q2)), k2)
                    sc = sc + (_wide(cc_ref[h]) - cr_ref[h:h + 1, :])
                    if on_diagonal:
                        sc = jnp.where(causal, sc, NEG)
                    m_prev = m_sc[h]
                    m_new = jnp.maximum(m_prev, jnp.max(sc, axis=1, keepdims=True))
                    alpha = jnp.exp(m_prev - m_new)
                    pm = jnp.exp(sc - _wide(m_new))
                    l_sc[h] = alpha * l_sc[h] + jnp.sum(pm, axis=1, keepdims=True)
                    m_sc[h] = m_new
                    pv = jnp.dot(pm.astype(BF16), v2, preferred_element_type=F32)
                    acc2 = jnp.where(msk, alpha * acc2 + pv, acc2)
                acc_sc[:, lanes] = acc2

        pl.when(j < i)(functools.partial(tile, False))
        pl.when(j == i)(functools.partial(tile, True))

        @pl.when(j == i)
        def _():
            left = lax.broadcasted_iota(jnp.int32, (1, LANES), 1) < HEAD
            for p in range(N_PAIRS):
                lanes = slice(p * LANES, (p + 1) * LANES)
                inv = jnp.where(left, 1.0 / l_sc[2 * p], 1.0 / l_sc[2 * p + 1])
                o = acc_sc[:, lanes] * inv
                o_ref[:, lanes] = o
                gate = g_ref[:, lanes]
                mix_ref[:, lanes] = o * (gate * _sigmoid(gate))
            for h in range(N_HEADS):
                lse_ref[h] = m_sc[h] + jnp.log(l_sc[h])

    qblk = pl.BlockSpec((t, D_HALF), lambda i, j: (i, 0))
    kblk = pl.BlockSpec((t, D_HALF), lambda i, j: (jnp.minimum(i, j), 0))
    return pl.pallas_call(
        body, name="fox_attn_fwd", grid=(nblk, nblk),
        in_specs=[qblk, kblk, kblk, pl.BlockSpec((N_HEADS, t, LANES), lambda i, j: (0, i, 0)),
                  pl.BlockSpec((N_HEADS, t), lambda i, j: (0, jnp.minimum(i, j))),
                  pl.BlockSpec((t, D_HALF), lambda i, j: (i, 3))],
        out_specs=[qblk, qblk, pl.BlockSpec((N_HEADS, t, LANES), lambda i, j: (0, i, 0))],
        out_shape=[jax.ShapeDtypeStruct((s, D_HALF), F32), jax.ShapeDtypeStruct((s, D_HALF), F32),
                   jax.ShapeDtypeStruct((N_HEADS, s, LANES), F32)],
        scratch_shapes=[pltpu.VMEM((N_HEADS, t, LANES), F32), pltpu.VMEM((N_HEADS, t, LANES), F32),
                        pltpu.VMEM((t, D_HALF), F32)],
        compiler_params=_params("parallel", "arbitrary"),
    )(q, k, v, cc, cr, u_b)


def _fox_post_bwd(dmix, o, u_b, tm=256):
    s = o.shape[0]

    def body(dm_ref, o_ref, g_ref, do_ref, dg_ref):
        gate = g_ref[...]
        sg = _sigmoid(gate)
        dm = dm_ref[...]
        do_ref[...] = (dm * (gate * sg)).astype(BF16)
        dg_ref[...] = dm * o_ref[...] * (sg * (1.0 + gate * (1.0 - sg)))

    blk = pl.BlockSpec((tm, D_HALF), lambda i: (i, 0))
    return pl.pallas_call(
        body, name="fox_post_bwd", grid=(s // tm,),
        in_specs=[blk, blk, pl.BlockSpec((tm, D_HALF), lambda i: (i, 3))], out_specs=[blk] * 2,
        out_shape=[jax.ShapeDtypeStruct((s, D_HALF), BF16), jax.ShapeDtypeStruct((s, D_HALF), F32)],
        compiler_params=_params("parallel"),
    )(dmix, o, u_b)


def _causal_tile(t):
    return lax.broadcasted_iota(jnp.int32, (t, t), 0) >= lax.broadcasted_iota(jnp.int32, (t, t), 1)


def _wide(x):
    return jnp.concatenate([x, x], axis=1)


def _attn_probs(q2, k2, v2, do2, msk, causal, bias, lse_rows):
    zero = jnp.zeros_like(q2)
    qh = jnp.where(msk, q2, zero)
    doh = jnp.where(msk, do2, zero)
    sc = _dot_nt(qh, k2) + bias
    if causal is not None:
        sc = jnp.where(causal, sc, NEG)
    pm = jnp.exp(sc - _wide(lse_rows))
    dp = _dot_nt(doh, v2)
    return qh, doh, pm, dp


def _attn_bwd_rowdot(q, k, v, do, lse, cc, cr):
    s = q.shape[0]
    t = ATT_T
    nblk = s // t

    def body(q_ref, k_ref, v_ref, do_ref, lse_ref, cc_ref, cr_ref, dd_ref, acc):
        i = pl.program_id(0)
        j = pl.program_id(1)

        @pl.when(j == 0)
        def _():
            acc[...] = jnp.zeros_like(acc)

        def tile(on_diagonal):
            causal = _causal_tile(t) if on_diagonal else None
            left = lax.broadcasted_iota(jnp.int32, (1, LANES), 1) < HEAD
            for p in range(N_PAIRS):
                lanes = slice(p * LANES, (p + 1) * LANES)
                q2, k2, v2, do2 = q_ref[:, lanes], k_ref[:, lanes], v_ref[:, lanes], do_ref[:, lanes]
                for e in range(2):
                    h = 2 * p + e
                    msk = left if e == 0 else jnp.logical_not(left)
                    bias = _wide(cc_ref[h]) - cr_ref[h:h + 1, :]
                    _, _, pm, dp = _attn_probs(q2, k2, v2, do2, msk, causal, bias, lse_ref[h])
                    acc[h] += jnp.sum(pm * dp, axis=1, keepdims=True)

        pl.when(j < i)(functools.partial(tile, False))
        pl.when(j == i)(functools.partial(tile, True))

        @pl.when(j == i)
        def _():
            dd_ref[...] = acc[...]

    qblk = pl.BlockSpec((t, D_HALF), lambda i, j: (i, 0))
    qcol = pl.BlockSpec((N_HEADS, t, LANES), lambda i, j: (0, i, 0))
    kblk = pl.BlockSpec((t, D_HALF), lambda i, j: (jnp.minimum(i, j), 0))
    return pl.pallas_call(
        body, name="fox_attn_rowdot", grid=(nblk, nblk),
        in_specs=[qblk, kblk, kblk, qblk, qcol, qcol, pl.BlockSpec((N_HEADS, t), lambda i, j: (0, jnp.minimum(i, j)))],
        out_specs=qcol, out_shape=jax.ShapeDtypeStruct((N_HEADS, s, LANES), F32),
        scratch_shapes=[pltpu.VMEM((N_HEADS, t, LANES), F32)],
        compiler_params=_params("parallel", "arbitrary"),
    )(q, k, v, do, lse, cc, cr)


def _attn_bwd(q, k, v, do, lse, dd, cc, cr):
    s = q.shape[0]
    t = ATT_T
    nblk = s // t

    def body(q_ref, k_ref, v_ref, do_ref, lse_ref, dd_ref, cc_ref, cr_ref,
             dq_ref, dk_ref, dv_ref, dcr_ref, dk_sc, dv_sc, dcr_sc):
        j = pl.program_id(0)
        i = pl.program_id(1)

        @pl.when(jnp.logical_and(j == 0, i == 0))
        def _():
            dq_ref[...] = jnp.zeros_like(dq_ref)

        @pl.when(i == 0)
        def _():
            dk_sc[...] = jnp.zeros_like(dk_sc)
            dv_sc[...] = jnp.zeros_like(dv_sc)
            dcr_sc[...] = jnp.zeros_like(dcr_sc)

        def tile(on_diagonal):
            causal = _causal_tile(t) if on_diagonal else None
            left = lax.broadcasted_iota(jnp.int32, (1, LANES), 1) < HEAD
            qrows = pl.ds(pl.multiple_of(i * t, t), t)
            for p in range(N_PAIRS):
                lanes = slice(p * LANES, (p + 1) * LANES)
                q2, k2, v2, do2 = q_ref[:, lanes], k_ref[:, lanes], v_ref[:, lanes], do_ref[:, lanes]
                zero = jnp.zeros_like(q2)
                dq2 = jnp.zeros((t, LANES), F32)
                dk2 = jnp.zeros((t, LANES), F32)
                dv2 = jnp.zeros((t, LANES), F32)
                for e in range(2):
                    h = 2 * p + e
                    msk = left if e == 0 else jnp.logical_not(left)
                    bias = _wide(cc_ref[h]) - cr_ref[h:h + 1, :]
                    qh, doh, pm, dp = _attn_probs(q2, k2, v2, do2, msk, causal, bias, lse_ref[h])
                    dsc = pm * (dp - _wide(dd_ref[h]))
                    dsb = dsc.astype(BF16)
                    dv2 += _dot_tn(pm.astype(BF16), doh)
                    dk2 += _dot_tn(dsb, qh)
                    dq2 += jnp.dot(dsb, jnp.where(msk, k2, zero), preferred_element_type=F32)
                    dcr_sc[h:h + 1, :] += -_colsum(dsc)
                dq_ref[qrows, lanes] += dq2 * ATT_SCALE
                dk_sc[:, lanes] += dk2
                dv_sc[:, lanes] += dv2

        pl.when(i > j)(functools.partial(tile, False))
        pl.when(i == j)(functools.partial(tile, True))

        @pl.when(i == nblk - 1)
        def _():
            dk_ref[...] = dk_sc[...]
            dv_ref[...] = dv_sc[...]
            dcr_ref[...] = dcr_sc[...]

    qblk = pl.BlockSpec((t, D_HALF), lambda j, i: (jnp.maximum(i, j), 0))
    qcol = pl.BlockSpec((N_HEADS, t, LANES), lambda j, i: (0, jnp.maximum(i, j), 0))
    kblk = pl.BlockSpec((t, D_HALF), lambda j, i: (j, 0))
    return pl.pallas_call(
        body, name="fox_attn_bwd", grid=(nblk, nblk),
        in_specs=[qblk, kblk, kblk, qblk, qcol, qcol, qcol, pl.BlockSpec((N_HEADS, t), lambda j, i: (0, j))],
        out_specs=[pl.BlockSpec((s, D_HALF), lambda j, i: (0, 0)), kblk, kblk,
                   pl.BlockSpec((N_HEADS, t), lambda j, i: (0, j))],
        out_shape=[jax.ShapeDtypeStruct((s, D_HALF), F32)] * 3 + [jax.ShapeDtypeStruct((N_HEADS, s), F32)],
        scratch_shapes=[pltpu.VMEM((t, D_HALF), F32), pltpu.VMEM((t, D_HALF), F32), pltpu.VMEM((N_HEADS, t), F32)],
        compiler_params=_params("arbitrary", "arbitrary"),
    )(q, k, v, do, lse, dd, cc, cr)


def _fox_prep_bwd(u_b, dq, dk, dv, dgate, dcum, fb, qg, kg, tm=256):
    s = u_b.shape[0]
    nb = s // tm

    def body(ub_ref, dq_ref, dk_ref, dv_ref, dg_ref, dc_ref, fb_ref, qg_ref, kg_ref,
             du_ref, dqg_ref, dkg_ref, dfb_ref, carry):
        i = pl.program_id(0)

        @pl.when(i == 0)
        def _():
            carry[...] = jnp.zeros_like(carry)
            dqg_ref[...] = jnp.zeros_like(dqg_ref)
            dkg_ref[...] = jnp.zeros_like(dkg_ref)
            dfb_ref[...] = jnp.zeros_like(dfb_ref)

        bd = _head_ones()
        for lo, g_ref, d_ref, dgain_ref in ((0, qg_ref, dq_ref, dqg_ref), (512, kg_ref, dk_ref, dkg_ref)):
            gain = g_ref[...]
            xh, rinv, _ = _head_rms(ub_ref[:, lo:lo + 512], gain, bd)
            dn = d_ref[...]
            dgain_ref[...] += _colsum(dn * xh)
            dxh = dn * gain
            du_ref[:, lo:lo + 512] = rinv * (dxh - xh * (_head_sum(dxh * xh, bd) * (1.0 / HEAD)))
        du_ref[:, 1024:1536] = dv_ref[...]
        du_ref[:, 1536:2048] = dg_ref[...]
        lane = lax.broadcasted_iota(jnp.int32, (1, LANES), 1)
        dc = dc_ref[...]
        dlogf = _exact_dot(dc, _tri(tm, False), ones_first=True) + carry[...]
        carry[...] += _colsum(dc)
        fl = ub_ref[:, 2048:2176] + fb_ref[...]
        dfl = jnp.where(lane < N_HEADS, dlogf * (1.0 - _sigmoid(fl)), 0.0)
        du_ref[:, 2048:2176] = dfl
        dfb_ref[...] += _colsum(dfl)

    rev = lambda w: pl.BlockSpec((tm, w), lambda i: (nb - 1 - i, 0))
    vec = lambda w: pl.BlockSpec((1, w), lambda i: (0, 0))
    return pl.pallas_call(
        body, name="fox_prep_bwd", grid=(nb,),
        in_specs=[rev(SEC)] + [rev(D_HALF)] * 4 + [rev(LANES), vec(LANES), vec(D_HALF), vec(D_HALF)],
        out_specs=[rev(SEC), vec(D_HALF), vec(D_HALF), vec(LANES)],
        out_shape=[jax.ShapeDtypeStruct((s, SEC), F32), jax.ShapeDtypeStruct((1, D_HALF), F32),
                   jax.ShapeDtypeStruct((1, D_HALF), F32), jax.ShapeDtypeStruct((1, LANES), F32)],
        scratch_shapes=[pltpu.VMEM((1, LANES), F32)],
        compiler_params=_params("arbitrary"),
    )(u_b, dq, dk, dv, dgate, dcum, fb, qg, kg)


def _merge(mix_a, mix_b, u_g, x, tgt, wa, wb, wo, fg, tm=256):
    s, d = x.shape

    def body(ma_ref, mb_ref, ug_ref, x_ref, t_ref, wa_ref, wb_ref, wo_ref, fg_ref,
             dx2_ref, dma_ref, dmb_ref, dug_ref, dwa_ref, dwb_ref, dwo_ref, dfg_ref, loss_ref):
        i = pl.program_id(0)

        @pl.when(i == 0)
        def _():
            for ref in (dwa_ref, dwb_ref, dwo_ref, dfg_ref, loss_ref):
                ref[...] = jnp.zeros_like(ref)

        wa_v, wb_v, wo_v, fg_v = wa_ref[...], wb_ref[...], wo_ref[...], fg_ref[...]
        ma = ma_ref[...].astype(BF16)
        mb = mb_ref[...].astype(BF16)
        ya = jnp.dot(ma, wa_v, preferred_element_type=F32)
        yb = jnp.dot(mb, wb_v, preferred_element_type=F32)
        sa = _sigmoid(ug_ref[:, 0:d])
        sb = _sigmoid(ug_ref[:, d:2 * d])
        merged = (sa * ya + sb * yb).astype(BF16)
        x2 = x_ref[...] + jnp.dot(merged, wo_v, preferred_element_type=F32)
        r2 = lax.rsqrt(jnp.mean(x2 * x2, axis=-1, keepdims=True) + RMS_EPS)
        x2h = x2 * r2
        err = x2h * fg_v - t_ref[...]
        loss_ref[...] += _colsum(err * err)
        dy = err * (1.0 / d)
        dfg_ref[...] += _colsum(dy * x2h)
        dx2h = dy * fg_v
        dx2 = r2 * (dx2h - x2h * jnp.mean(dx2h * x2h, axis=-1, keepdims=True))
        dx2_ref[...] = dx2
        dx2b = dx2.astype(BF16)
        dmerged = _dot_nt(dx2b, wo_v)
        dwo_ref[...] += _dot_tn(merged, dx2b)
        dya = dmerged * sa
        dyb = dmerged * sb
        dug_ref[:, 0:d] = dya * ya * (1.0 - sa)
        dug_ref[:, d:2 * d] = dyb * yb * (1.0 - sb)
        dyab = dya.astype(BF16)
        dybb = dyb.astype(BF16)
        dma_ref[...] = _dot_nt(dyab, wa_v)
        dmb_ref[...] = _dot_nt(dybb, wb_v)
        dwa_ref[...] += _dot_tn(ma, dyab)
        dwb_ref[...] += _dot_tn(mb, dybb)

    row = lambda w: pl.BlockSpec((tm, w), lambda i: (i, 0))
    full = lambda a: pl.BlockSpec(a.shape, lambda i: (0, 0))
    fshape = lambda a: jax.ShapeDtypeStruct(a.shape, F32)
    return pl.pallas_call(
        body, name="merge_fwd_bwd", grid=(s // tm,),
        in_specs=[row(D_HALF), row(D_HALF), row(GATE_COLS), row(d), row(d), full(wa), full(wb), full(wo), full(fg)],
        out_specs=[row(d), row(D_HALF), row(D_HALF), row(GATE_COLS), full(wa), full(wb), full(wo), full(fg), full(fg)],
        out_shape=[jax.ShapeDtypeStruct((s, d), F32), jax.ShapeDtypeStruct((s, D_HALF), F32),
                   jax.ShapeDtypeStruct((s, D_HALF), F32), jax.ShapeDtypeStruct((s, GATE_COLS), F32),
                   fshape(wa), fshape(wb), fshape(wo), fshape(fg), fshape(fg)],
        compiler_params=_params("arbitrary"),
    )(mix_a, mix_b, u_g, x, tgt, wa, wb, wo, fg)


def _lora_weight(w_up, a_up):
    z = jnp.zeros((LORA, D_HALF), w_up.dtype)
    return jnp.concatenate([jnp.concatenate([w_up, z], axis=1), jnp.concatenate([z, a_up], axis=1)], axis=0)


def _device_grads(x, tgt, p, w_a, w_up, a_up, late_weights, fwd_exchange=None, bwd_exchange=None, tail_exchange=None):
    wl = _lora_weight(w_up, a_up)
    rk = p["r_k"].reshape(1, D_HALF)
    fb = jnp.pad(p["f_bias"], ((0, 0), (0, LANES - N_HEADS)))
    qg = jnp.tile(p["q_norm_g"], (1, N_HEADS))
    kg = jnp.tile(p["k_norm_g"], (1, N_HEADS))
    fg = p["final_norm_g"].reshape(1, D_MODEL)
    mixer = (p["shift_mu"], wl, p["w0"], p["a0"], p["k_k"], p["k_a"])

    h = _rmsnorm_in(x, p["norm_g"])
    u_a = _matmul_nn(h, w_a, "inproj_rwkv")
    r, dec, k, v, av, bv, gate_a = _rwkv_prep(u_a, *mixer)
    y, st, arrived = _wkv_fwd(r, dec, k, av, bv, v, fwd_exchange)
    mix_a = _rwkv_post(y, r, k, v, gate_a, p["lnx_w"], p["lnx_b"], rk)

    w_b, w_g, w_out_a, w_out_b, w_out = late_weights(arrived)
    u_b = _matmul_nn(h, w_b, "inproj_fox")
    u_g = _matmul_nn(h, w_g, "inproj_gate")
    q, kn, vb, cc, cr = _fox_prep(u_b, fb, qg, kg)
    o, mix_b, lse = _attn_fwd(q, kn, vb, cc, cr, u_b)

    dx2, dmix_a, dmix_b, du_g, dwa, dwb, dwo, dfg, loss_vec = _merge(
        mix_a, mix_b, u_g, x, tgt, w_out_a, w_out_b, w_out, fg)

    do, dgate_b = _fox_post_bwd(dmix_b, o, u_b)
    dd = _attn_bwd_rowdot(q, kn, vb, do, lse, cc, cr)
    dq, dk_att, dv_att, dcr = _attn_bwd(q, kn, vb, do, lse, dd, cc, cr)
    dcum = jnp.pad(dcr.T, ((0, 0), (0, LANES - N_HEADS)))
    du_b, dqg, dkg, dfb = _fox_prep_bwd(u_b, dq, dk_att, dv_att, dgate_b, dcum, fb, qg, kg)
    h_t = h.T
    dw_b = _matmul_tn_acc(h_t, du_b, "dw_fox")
    dw_g = _matmul_tn_acc(h_t, du_g, "dw_gate")

    dy, dr_b, dk_b, dv_b, dgate_a, dlw, dlb, drk = _rwkv_post_bwd(
        dmix_a, y, r, k, v, gate_a, p["lnx_w"], p["lnx_b"], rk)
    scan_grads, sent = _wkv_bwd(r, dec, k, av, bv, v, dy, st,
                                bwd_exchange(dw_b, dw_g, dwa, dwb, dwo) if bwd_exchange else None)
    du_a, dmu, dwl, dw0, da0, dkkw, dkaw = _rwkv_prep_bwd(u_a, (*scan_grads, dr_b, dk_b, dv_b, dgate_a), *mixer)
    dw_a = _matmul_tn_acc(h_t, du_a, "dw_rwkv")
    dw_up, da_up = dwl[:LORA, :D_HALF], dwl[LORA:, D_HALF:]
    sent_last = _run_on_sequencer(tail_exchange(dw_a, dw_up, da_up), "scatter_tail", 1) if tail_exchange else []
    grad_x, dnorm_g, _ = _inproj_bwd(du_a, du_b, du_g, w_a, w_b, w_g, x, dx2, p["norm_g"])

    grads = dict(
        norm_g=dnorm_g, w_in=(dw_a, dw_b, dw_g), shift_mu=dmu,
        w_lora_up=dw_up, w0=dw0, a_lora_up=da_up, a0=da0, k_k=dkkw, k_a=dkaw,
        r_k=drk.reshape(1, N_HEADS, HEAD), lnx_w=dlw, lnx_b=dlb, f_bias=dfb[:, :N_HEADS],
        q_norm_g=dqg.reshape(N_HEADS, HEAD).sum(axis=0, keepdims=True),
        k_norm_g=dkg.reshape(N_HEADS, HEAD).sum(axis=0, keepdims=True),
        w_out_a=dwa, w_out_b=dwb, w_out=dwo, final_norm_g=dfg.reshape(D_MODEL))
    return loss_vec, grad_x, grads, sent, sent_last


CHIP_FLIPS = ((1, 0), (0, 1), (1, 1))
ANY = pl.BlockSpec(memory_space=pl.ANY)


def _position():
    return lax.axis_index("x"), lax.axis_index("y"), lax.axis_index("c")


def _flip(v, f):
    return 1 - v if f else v


def _both(a, b):
    if a is None:
        return b
    return a if b is None else jnp.logical_and(a, b)


def _when(cond, fn):
    if cond is None:
        fn()
    else:
        pl.when(cond)(fn)


class _Moves:
    def __init__(self, send_sems, recv_sems, local_sems):
        self.send_sems, self.recv_sems, self.local_sems = send_sems, recv_sems, local_sems
        self.remote, self.local = [], []

    def send(self, src, dst, peer, landing, send_if=None, recv_if=None):
        k = len(self.remote)
        sems = dict(send_sem=self.send_sems.at[k], recv_sem=self.recv_sems.at[k], device_id=peer, device_id_type=MESH)
        out = pltpu.make_async_remote_copy(src_ref=src, dst_ref=dst, **sems)
        arrival = pltpu.make_async_remote_copy(src_ref=src, dst_ref=landing, **sems)
        self.remote.append((out, arrival, send_if, recv_if))

    def copy(self, src, dst, cond=None):
        cp = pltpu.make_async_copy(src, dst, self.local_sems.at[len(self.local)])
        self.local.append((cp, cond))

    def start(self, also=None):
        for cp, cond in self.local:
            _when(_both(also, cond), cp.start)
        for out, _, send_if, _ in self.remote:
            _when(_both(also, send_if), out.start)

    def wait_arrivals(self, also=None):
        for _, arrival, _, recv_if in self.remote:
            _when(_both(also, recv_if), arrival.wait_recv)

    def wait_sent(self, also=None):
        for out, _, send_if, _ in self.remote:
            _when(_both(also, send_if), out.wait_send)
        for cp, cond in self.local:
            _when(_both(also, cond), cp.wait)

    def wait(self, also=None):
        self.wait_arrivals(also)
        self.wait_sent(also)


class _Exchange:
    def __init__(self, operands, out_shapes, n_remote, n_local, build, n_relay=0, relay=None):
        self.operands, self.out_shapes = list(operands), list(out_shapes)
        self.n_remote, self.n_local, self.build = n_remote, n_local, build
        self.n_relay, self.relay = n_relay, relay

    def scratch(self):
        return [pltpu.SemaphoreType.DMA((self.n_remote,)), pltpu.SemaphoreType.DMA((self.n_remote,)),
                pltpu.SemaphoreType.DMA((max(self.n_local, 1),))]

    def moves(self, in_refs, out_refs, sems):
        mv = _Moves(*sems)
        self.build(mv, in_refs, out_refs)
        return mv

    def run_alone(self, name):
        n_in, n_out = len(self.operands), len(self.out_shapes)
        relay_scratch = [pltpu.SemaphoreType.DMA((self.n_relay,))] * 2 if self.relay else []

        def body(*refs):
            ins, outs, sems = refs[:n_in], refs[n_in:n_in + n_out], refs[n_in + n_out:]
            mv = self.moves(ins, outs, sems[:3])
            mv.start()
            mv.wait_arrivals()
            if self.relay:
                passed = _Moves(sems[3], sems[4], None)
                self.relay(passed, ins, outs)
                passed.start()
                passed.wait()
            mv.wait_sent()

        return pl.pallas_call(
            body, name=name, in_specs=[ANY] * n_in, out_specs=[ANY] * n_out, out_shape=self.out_shapes,
            scratch_shapes=self.scratch() + relay_scratch, compiler_params=pltpu.CompilerParams(has_side_effects=True),
        )(*self.operands)


def _run_on_sequencer(exchange, name, collective_id):
    ins = [jax.new_ref(a, memory_space=pltpu.MemorySpace.HBM) for a in exchange.operands]
    outs = [jax.empty_ref(s, memory_space=pltpu.MemorySpace.HBM) for s in exchange.out_shapes]

    def launch(send_sems, recv_sems, local_sems):
        x, y, c = _position()
        barrier = pltpu.get_barrier_semaphore()
        for fx, fy in CHIP_FLIPS:
            pl.semaphore_signal(barrier, inc=1, device_id=(_flip(x, fx), _flip(y, fy), c), device_id_type=MESH)
        pl.semaphore_wait(barrier, len(CHIP_FLIPS))
        moves = exchange.moves(ins, outs, (send_sems, recv_sems, local_sems))
        moves.start()
        moves.wait()

    pl.kernel(launch, mesh=plsc.ScalarSubcoreMesh(axis_name="sequencer", num_cores=1), name=name,
              scratch_types=tuple(exchange.scratch()),
              compiler_params=pltpu.CompilerParams(collective_id=collective_id))()
    return [o[...] for o in outs]


def _is_chip(x, y, chip):
    return jnp.logical_and(x == chip // 2, y == chip % 2)


def _gather_exchange(from_chip, from_all, split=()):
    n1, n2 = len(from_chip), len(from_all)

    def rows_of(t, c):
        half = from_chip[t][1].shape[0] // 2
        return pl.ds(c * half, half)

    def build(mv, ins, outs):
        x, y, c = _position()
        me = 2 * x + y
        for t, (chip, _) in enumerate(from_chip):
            mv.copy(ins[t], outs[t], cond=_is_chip(x, y, chip))
        for t in range(n2):
            mv.copy(ins[n1 + t], outs[n1 + t].at[me])
        for fx, fy in CHIP_FLIPS:
            px, py = _flip(x, fx), _flip(y, fy)
            peer = (px, py, c)
            for t, (chip, _) in enumerate(from_chip):
                part = rows_of(t, c) if t in split else slice(None)
                mv.send(ins[t].at[part], outs[t].at[part], peer, landing=outs[t].at[part],
                        send_if=_is_chip(x, y, chip), recv_if=_is_chip(px, py, chip))
            for t in range(n2):
                mv.send(ins[n1 + t], outs[n1 + t].at[me], peer, landing=outs[n1 + t].at[2 * px + py])

    def relay(mv, ins, outs):
        x, y, c = _position()
        for t in split:
            came = jnp.logical_not(_is_chip(x, y, from_chip[t][0]))
            mv.send(outs[t].at[rows_of(t, c)], outs[t].at[rows_of(t, c)], (x, y, 1 - c),
                    landing=outs[t].at[rows_of(t, 1 - c)], send_if=came, recv_if=came)

    arrays = [a for _, a in from_chip] + list(from_all)
    shapes = [jax.ShapeDtypeStruct(a.shape, a.dtype) for _, a in from_chip]
    shapes += [jax.ShapeDtypeStruct((N_CHIPS,) + a.shape, a.dtype) for a in from_all]
    return _Exchange(arrays, shapes, len(CHIP_FLIPS) * (n1 + n2), n1 + n2, build,
                     n_relay=len(split), relay=relay if split else None)


def _scatter_exchange(to_chip, to_all):
    n1, n2 = len(to_chip), len(to_all)

    def build(mv, ins, outs):
        x, y, c = _position()
        for f, (fx, fy) in enumerate(CHIP_FLIPS):
            px, py = _flip(x, fx), _flip(y, fy)
            peer = (px, py, c)
            for t, (chip, _) in enumerate(to_chip):
                mv.send(ins[t], outs[t].at[f], peer, landing=outs[t].at[f],
                        send_if=_is_chip(px, py, chip), recv_if=_is_chip(x, y, chip))
            for t in range(n2):
                mv.send(ins[n1 + t].at[2 * px + py], outs[n1 + t].at[f], peer, landing=outs[n1 + t].at[f])

    arrays = [a for _, a in to_chip] + list(to_all)
    shapes = [jax.ShapeDtypeStruct((len(CHIP_FLIPS),) + a.shape, a.dtype) for _, a in to_chip]
    shapes += [jax.ShapeDtypeStruct((len(CHIP_FLIPS),) + a.shape[1:], a.dtype) for a in to_all]
    return _Exchange(arrays, shapes, len(CHIP_FLIPS) * (n1 + n2), 0, build)


def _swap_sibling(tensors):
    n = len(tensors)

    def body(*refs):
        ins, outs = refs[:n], refs[n:2 * n]
        send_sems, recv_sems = refs[2 * n:]
        x, y, c = _position()
        copies = [pltpu.make_async_remote_copy(
            src_ref=ins[t], dst_ref=outs[t], send_sem=send_sems.at[t], recv_sem=recv_sems.at[t],
            device_id=(x, y, 1 - c), device_id_type=MESH) for t in range(n)]
        for cp in copies:
            cp.start()
        for cp in copies:
            cp.wait_recv()
        for cp in copies:
            cp.wait_send()

    return pl.pallas_call(
        body, name="swap_sibling", in_specs=[ANY] * n, out_specs=[ANY] * n,
        out_shape=[jax.ShapeDtypeStruct(a.shape, a.dtype) for a in tensors],
        scratch_shapes=[pltpu.SemaphoreType.DMA((n,)), pltpu.SemaphoreType.DMA((n,))],
        compiler_params=pltpu.CompilerParams(has_side_effects=True),
    )(*tensors)


def _allreduce_small(slab):
    stages = 3

    def body(x_ref, o_ref, buf, send_sems, recv_sems):
        x, y, c = _position()
        peers = ((1 - x, y, c), (x, 1 - y, c), (x, y, 1 - c))
        o_ref[...] = x_ref[...]
        for k, peer in enumerate(peers):
            cp = pltpu.make_async_remote_copy(src_ref=o_ref, dst_ref=buf.at[k], send_sem=send_sems.at[k],
                                              recv_sem=recv_sems.at[k], device_id=peer, device_id_type=MESH)
            cp.start()
            cp.wait()
            o_ref[...] = o_ref[...] + buf[k]

    return pl.pallas_call(
        body, name="allreduce_small",
        in_specs=[pl.BlockSpec(memory_space=pltpu.VMEM)], out_specs=pl.BlockSpec(memory_space=pltpu.VMEM),
        out_shape=jax.ShapeDtypeStruct(slab.shape, slab.dtype),
        scratch_shapes=[pltpu.VMEM((stages,) + slab.shape, slab.dtype),
                        pltpu.SemaphoreType.DMA((stages,)), pltpu.SemaphoreType.DMA((stages,))],
        compiler_params=pltpu.CompilerParams(has_side_effects=True),
    )(slab)


def _row_tile(r):
    return min(r, 256)


def _sum4(stack, recv, me):
    _, r, c = stack.shape
    tr = _row_tile(r)

    def body(me_ref, own_ref, recv_ref, o_ref):
        o_ref[...] = (((own_ref[...] + recv_ref[0].astype(F32)) + recv_ref[1].astype(F32))
                      + recv_ref[2].astype(F32))

    return pl.pallas_call(
        body, name="sum_partials",
        grid_spec=pltpu.PrefetchScalarGridSpec(
            num_scalar_prefetch=1, grid=(r // tr,),
            in_specs=[pl.BlockSpec((None, tr, c), lambda i, me_ref: (me_ref[0], i, 0)),
                      pl.BlockSpec((len(CHIP_FLIPS), tr, c), lambda i, me_ref: (0, i, 0))],
            out_specs=pl.BlockSpec((tr, c), lambda i, me_ref: (i, 0))),
        out_shape=jax.ShapeDtypeStruct((r, c), F32), compiler_params=_params("parallel"),
    )(me, stack, recv)


def _sum_block(own, recv):
    r, c = own.shape
    tr = _row_tile(r)

    def body(own_ref, recv_ref, o_ref):
        o_ref[...] = (((own_ref[...] + recv_ref[0].astype(F32)) + recv_ref[1].astype(F32))
                      + recv_ref[2].astype(F32))

    return pl.pallas_call(
        body, name="sum_block", grid=(r // tr,),
        in_specs=[pl.BlockSpec((tr, c), lambda i: (i, 0)), pl.BlockSpec((len(CHIP_FLIPS), tr, c), lambda i: (0, i, 0))],
        out_specs=pl.BlockSpec((tr, c), lambda i: (i, 0)),
        out_shape=jax.ShapeDtypeStruct((r, c), F32), compiler_params=_params("parallel"),
    )(own, recv)


def _adamw_math(w, g, m, v):
    m = ADAM_B1 * m + (1.0 - ADAM_B1) * g
    v = ADAM_B2 * v + (1.0 - ADAM_B2) * (g * g)
    m_hat = m / (1.0 - ADAM_B1 ** ADAM_STEP)
    v_hat = v / (1.0 - ADAM_B2 ** ADAM_STEP)
    delta = -ADAM_LR * (m_hat / (jnp.sqrt(v_hat) + ADAM_EPS) + ADAM_WD * w)
    return delta, m, v


def _adamw(w, m, v, g_parts, name):
    r, c = w.shape
    tr = _row_tile(r)
    n = len(g_parts)

    def body(*refs):
        w_ref, m_ref, v_ref = refs[:3]
        g_refs = refs[3:3 + n]
        g_out, d_out, m_out, v_out = refs[3 + n:]
        g = g_refs[0][...]
        for ref in g_refs[1:]:
            g = g + ref[...]
        g_out[...] = g
        d_out[...], m_out[...], v_out[...] = _adamw_math(w_ref[...], g, m_ref[...], v_ref[...])

    blk = pl.BlockSpec((tr, c), lambda i: (i, 0))
    return pl.pallas_call(
        body, name=name, grid=(r // tr,), in_specs=[blk] * (3 + n), out_specs=[blk] * 4,
        out_shape=[jax.ShapeDtypeStruct((r, c), F32)] * 4, compiler_params=_params("parallel"),
    )(w, m, v, *g_parts)


SHARDED = ("w_in", "w_lora_up", "a_lora_up", "w_out_a", "w_out_b", "w_out")
ROW_SHARDED = ("w_out",)
SMALL = ("norm_g", "shift_mu", "w0", "a0", "k_k", "k_a", "r_k", "lnx_w", "lnx_b", "f_bias", "q_norm_g", "k_norm_g",
         "final_norm_g")
WEIGHTS = ("norm_g", "w_in", "shift_mu", "w_lora_up", "w0", "a_lora_up", "a0", "k_k", "k_a", "r_k", "lnx_w", "lnx_b",
           "f_bias", "q_norm_g", "k_norm_g", "w_out_a", "w_out_b", "w_out", "final_norm_g")
SLAB_ROWS = 16
SLAB_COLS = SEC


def _to_slab(named, extra=None):
    rows = [jnp.pad(named[n].reshape(1, -1), ((0, 0), (0, SLAB_COLS - named[n].size))) for n in SMALL]
    if extra is not None:
        rows.append(jnp.pad(extra.reshape(1, -1), ((0, 0), (0, SLAB_COLS - extra.size))))
    rows.append(jnp.zeros((SLAB_ROWS - len(rows), SLAB_COLS), F32))
    return jnp.concatenate(rows, axis=0)


def _from_slab(slab, shapes):
    return {n: slab[i, :math.prod(shapes[n])].reshape(shapes[n]) for i, n in enumerate(SMALL)}


def _by_chip(g, name):
    if name in ROW_SHARDED:
        return g.reshape(N_CHIPS, g.shape[0] // N_CHIPS, g.shape[1])
    r, c = g.shape
    return g.reshape(r, N_CHIPS, c // N_CHIPS).transpose(1, 0, 2)


def _from_chips(stack, name):
    if name in ROW_SHARDED:
        return stack.reshape(-1, stack.shape[2])
    _, r, c = stack.shape
    return stack.transpose(1, 0, 2).reshape(r, N_CHIPS * c)


def kernel(x, norm_g, w_in, shift_mu, w_lora_up, w0, a_lora_up, a0, k_k, k_a, r_k, lnx_w, lnx_b, f_bias, q_norm_g, k_norm_g, w_out_a, w_out_b, w_out, final_norm_g, loss_target, m_norm_g, m_w_in, m_shift_mu, m_w_lora_up, m_w0, m_a_lora_up, m_a0, m_k_k, m_k_a, m_r_k, m_lnx_w, m_lnx_b, m_f_bias, m_q_norm_g, m_k_norm_g, m_w_out_a, m_w_out_b, m_w_out, m_final_norm_g, v_norm_g, v_w_in, v_shift_mu, v_w_lora_up, v_w0, v_a_lora_up, v_a0, v_k_k, v_k_a, v_r_k, v_lnx_w, v_lnx_b, v_f_bias, v_q_norm_g, v_k_norm_g, v_w_out_a, v_w_out_b, v_w_out, v_final_norm_g):
    w = dict(norm_g=norm_g, w_in=w_in, shift_mu=shift_mu, w_lora_up=w_lora_up, w0=w0, a_lora_up=a_lora_up, a0=a0,
             k_k=k_k, k_a=k_a, r_k=r_k, lnx_w=lnx_w, lnx_b=lnx_b, f_bias=f_bias, q_norm_g=q_norm_g,
             k_norm_g=k_norm_g, w_out_a=w_out_a, w_out_b=w_out_b, w_out=w_out, final_norm_g=final_norm_g)
    m = dict(norm_g=m_norm_g, w_in=m_w_in, shift_mu=m_shift_mu, w_lora_up=m_w_lora_up, w0=m_w0,
             a_lora_up=m_a_lora_up, a0=m_a0, k_k=m_k_k, k_a=m_k_a, r_k=m_r_k, lnx_w=m_lnx_w, lnx_b=m_lnx_b,
             f_bias=m_f_bias, q_norm_g=m_q_norm_g, k_norm_g=m_k_norm_g, w_out_a=m_w_out_a, w_out_b=m_w_out_b,
             w_out=m_w_out, final_norm_g=m_final_norm_g)
    v = dict(norm_g=v_norm_g, w_in=v_w_in, shift_mu=v_shift_mu, w_lora_up=v_w_lora_up, w0=v_w0,
             a_lora_up=v_a_lora_up, a0=v_a0, k_k=v_k_k, k_a=v_k_a, r_k=v_r_k, lnx_w=v_lnx_w, lnx_b=v_lnx_b,
             f_bias=v_f_bias, q_norm_g=v_q_norm_g, k_norm_g=v_k_norm_g, w_out_a=v_w_out_a, w_out_b=v_w_out_b,
             w_out=v_w_out, final_norm_g=v_final_norm_g)
    shapes = {n: w[n].shape for n in WEIGHTS}

    shard = {n: w[n][0].astype(BF16) for n in SHARDED}
    late = ("w_out_a", "w_out_b", "w_out")
    loras = ("w_lora_up", "a_lora_up")
    w_in_head, w_in_tail = shard["w_in"][:, :A_TAIL], shard["w_in"][:, A_TAIL:]
    shard0, shard1_head, up_stack, aup_stack = _gather_exchange(
        [(0, shard["w_in"]), (1, w_in_head)], [shard[n] for n in loras], split=(0,)).run_alone("gather_early")
    w_a = jnp.concatenate([shard0, shard1_head], axis=1)

    def late_weights(arrived):
        shard1_tail, shard2, shard3 = arrived[:3]
        w_b = jnp.concatenate([shard1_tail, shard2[:, :B_TAIL], jnp.zeros((D_MODEL, SEC - FOX_REAL), BF16)], axis=1)
        w_g = jnp.concatenate([shard2[:, B_TAIL:], shard3], axis=1)
        return (w_b, w_g, *[_from_chips(s, n) for n, s in zip(late, arrived[3:])])

    own = {}

    def bwd_exchange(dw_b, dw_g, dwa, dwb, dwo):
        own["tail1"] = dw_b[:, :B_HEAD]
        own["block2"] = jnp.concatenate([dw_b[:, B_HEAD:FOX_REAL], dw_g[:, :G_HEAD]], axis=1)
        own["block3"] = dw_g[:, G_HEAD:]
        own.update({n: _by_chip(g, n) for n, g in zip(late, (dwa, dwb, dwo))})
        return _scatter_exchange([(1, own["tail1"].astype(BF16)), (2, own["block2"].astype(BF16)),
                                  (3, own["block3"].astype(BF16))], [own[n].astype(BF16) for n in late])

    def tail_exchange(dw_a, dw_up, da_up):
        own["block0"], own["head1"] = dw_a[:, :SHARD_COLS], dw_a[:, SHARD_COLS:]
        own.update({n: _by_chip(g, n) for n, g in zip(loras, (dw_up, da_up))})
        return _scatter_exchange([(0, own["block0"].astype(BF16)), (1, own["head1"].astype(BF16))],
                                 [own[n].astype(BF16) for n in loras])

    small = {n: w[n] for n in SMALL}
    loss_vec, grad_x, grads, sent, sent_last = _device_grads(
        x[0], loss_target[0], small, w_a, _from_chips(up_stack, "w_lora_up"), _from_chips(aup_stack, "a_lora_up"),
        late_weights, _gather_exchange([(1, w_in_tail), (2, shard["w_in"]), (3, shard["w_in"])], [shard[n] for n in late]),
        bwd_exchange, tail_exchange)

    total = _allreduce_small(_to_slab(grads, extra=loss_vec))
    loss = (0.5 / D_MODEL) * jnp.sum(total[len(SMALL)])
    slab_g, slab_d, slab_m, slab_v = _adamw(_to_slab(w), _to_slab(m), _to_slab(v), [total], "adamw_small")
    out_g, out_d, out_m, out_v = (_from_slab(s, shapes) for s in (total, slab_d, slab_m, slab_v))
    del slab_g

    xpos, ypos, _ = _position()
    me = (2 * xpos + ypos).astype(jnp.int32).reshape(1)
    core_sum = {"w_in": lax.switch(me[0], [
        lambda: _sum_block(own["block0"], sent_last[0]),
        lambda: jnp.concatenate([_sum_block(own["head1"], sent_last[1]), _sum_block(own["tail1"], sent[0])], axis=1),
        lambda: _sum_block(own["block2"], sent[1]),
        lambda: _sum_block(own["block3"], sent[2])])}
    core_sum.update({n: _sum4(own[n], r, me) for n, r in zip(late, sent[3:])})
    core_sum.update({n: _sum4(own[n], r, me) for n, r in zip(loras, sent_last[2:])})
    sibling_sums = _swap_sibling([core_sum[n] for n in SHARDED])
    for n, theirs in zip(SHARDED, sibling_sums):
        g, d, m2, v2 = _adamw(w[n][0], m[n][0], v[n][0], [core_sum[n], theirs], "adamw_" + n)
        out_g[n], out_d[n], out_m[n], out_v[n] = (a.reshape(shapes[n]) for a in (g, d, m2, v2))

    return (loss, grad_x.reshape(x.shape), *[out_g[n] for n in WEIGHTS], *[out_d[n] for n in WEIGHTS],
            *[out_m[n] for n in WEIGHTS], *[out_v[n] for n in WEIGHTS])
```

```python
import functools
import math

import jax
import jax.numpy as jnp
from jax import lax
from jax.experimental import pallas as pl
from jax.experimental.pallas import tpu as pltpu
from jax.experimental.pallas import tpu_sc as plsc

F32 = jnp.float32
BF16 = jnp.bfloat16

D_MODEL = 1024
D_HALF = 512
HEAD = 64
N_HEADS = 8
LORA = 64
RWKV_COLS = 2176
FOX_REAL = 2056
SEC = 2176
GATE_COLS = 2048
IN_COLS = 6280
N_CHIPS = 4
SHARD_COLS = IN_COLS // N_CHIPS
A_TAIL = RWKV_COLS - SHARD_COLS
B_HEAD = SHARD_COLS - A_TAIL
B_TAIL = FOX_REAL - B_HEAD
G_HEAD = SHARD_COLS - B_TAIL
RMS_EPS = 1e-6
LNX_EPS = 64e-5
ATT_SCALE = HEAD ** -0.5
NEG = -1e30

ADAM_LR = 0.001
ADAM_B1 = 0.9
ADAM_B2 = 0.999
ADAM_EPS = 1e-08
ADAM_WD = 0.01
ADAM_STEP = 10

LANES = 128
SUBLANES = 8
VMEM_LIMIT = 56 * 1024 * 1024
MESH = pl.DeviceIdType.MESH


def _params(*sem):
    return pltpu.CompilerParams(dimension_semantics=sem if sem else None, vmem_limit_bytes=VMEM_LIMIT)


def _sigmoid(x):
    return 1.0 / (1.0 + jnp.exp(-x))


def _log_sigmoid(x):
    return jnp.minimum(x, 0.0) - jnp.log(1.0 + jnp.exp(-jnp.abs(x)))


def _head_ones():
    r = lax.broadcasted_iota(jnp.int32, (LANES, LANES), 0) >> 6
    c = lax.broadcasted_iota(jnp.int32, (LANES, LANES), 1) >> 6
    return (r == c).astype(BF16)


def _split3(x):
    hi = x.astype(BF16)
    r1 = x - hi.astype(F32)
    mid = r1.astype(BF16)
    lo = (r1 - mid.astype(F32)).astype(BF16)
    return hi, mid, lo


def _exact_dot(x, ones_bf16, ones_first=False):
    out = None
    for piece in _split3(x):
        if ones_first:
            t = jnp.dot(ones_bf16, piece, preferred_element_type=F32)
        else:
            t = jnp.dot(piece, ones_bf16, preferred_element_type=F32)
        out = t if out is None else out + t
    return out


def _head_sum(x, bd):
    n = x.shape[1] // LANES
    parts = [_exact_dot(x[:, i * LANES:(i + 1) * LANES], bd) for i in range(n)]
    return parts[0] if n == 1 else jnp.concatenate(parts, axis=1)


def _dot_nt(a, b):
    return lax.dot_general(a, b, (((1,), (1,)), ((), ())), preferred_element_type=F32)


def _dot_tn(a, b):
    return lax.dot_general(a, b, (((0,), (0,)), ((), ())), preferred_element_type=F32)


def _colsum(x):
    return jnp.sum(x, axis=0, keepdims=True)


def _rmsnorm_in(x, g, tm=512):
    s, d = x.shape

    def body(x_ref, g_ref, h_ref):
        xv = x_ref[...]
        r = lax.rsqrt(jnp.mean(xv * xv, axis=-1, keepdims=True) + RMS_EPS)
        h_ref[...] = (xv * r * g_ref[...]).astype(BF16)

    return pl.pallas_call(
        body, name="rmsnorm_in", grid=(s // tm,),
        in_specs=[pl.BlockSpec((tm, d), lambda i: (i, 0)), pl.BlockSpec((1, d), lambda i: (0, 0))],
        out_specs=pl.BlockSpec((tm, d), lambda i: (i, 0)),
        out_shape=jax.ShapeDtypeStruct((s, d), BF16), compiler_params=_params("parallel"),
    )(x, g)


def _matmul_nn(a, b, name, tm=512):
    m, k = a.shape
    n = b.shape[1]

    def body(a_ref, b_ref, o_ref):
        o_ref[...] = jnp.dot(a_ref[...], b_ref[...], preferred_element_type=F32)

    return pl.pallas_call(
        body, name=name, grid=(m // tm,),
        in_specs=[pl.BlockSpec((tm, k), lambda i: (i, 0)), pl.BlockSpec((k, n), lambda i: (0, 0))],
        out_specs=pl.BlockSpec((tm, n), lambda i: (i, 0)),
        out_shape=jax.ShapeDtypeStruct((m, n), F32), compiler_params=_params("parallel"),
    )(a, b)


def _matmul_tn_acc(at, b, name, tk=512):
    m, k = at.shape
    n = b.shape[1]

    def body(a_ref, b_ref, o_ref):
        j = pl.program_id(0)

        @pl.when(j == 0)
        def _():
            o_ref[...] = jnp.zeros_like(o_ref)

        o_ref[...] += jnp.dot(a_ref[...], b_ref[...].astype(BF16), preferred_element_type=F32)

    return pl.pallas_call(
        body, name=name, grid=(k // tk,),
        in_specs=[pl.BlockSpec((m, tk), lambda j: (0, j)), pl.BlockSpec((tk, n), lambda j: (j, 0))],
        out_specs=pl.BlockSpec((m, n), lambda j: (0, 0)),
        out_shape=jax.ShapeDtypeStruct((m, n), F32), compiler_params=_params("arbitrary"),
    )(at, b)


def _inproj_bwd(du_a, du_b, du_g, w_a, w_b, w_g, x, dx2, g, exchange=None, tm=256):
    s, d = x.shape
    nb = s // tm

    def body(*refs):
        ((da_ref, db_ref, dg_ref, wa_ref, wb_ref, wg_ref, x_ref, dx2_ref, g_ref), (gx_ref, gg_ref), _,
         moves) = _split_refs(refs, 9, 2, exchange)
        i = pl.program_id(0)
        if moves:
            moves.start(also=(i == 0))

        @pl.when(i == 0)
        def _():
            gg_ref[...] = jnp.zeros_like(gg_ref)

        dh = _dot_nt(da_ref[...].astype(BF16), wa_ref[...])
        dh += _dot_nt(db_ref[...].astype(BF16), wb_ref[...])
        dh += _dot_nt(dg_ref[...].astype(BF16), wg_ref[...])
        xv = x_ref[...]
        r = lax.rsqrt(jnp.mean(xv * xv, axis=-1, keepdims=True) + RMS_EPS)
        xh = xv * r
        gg_ref[...] += _colsum(dh * xh)
        dxh = dh * g_ref[...]
        gx_ref[...] = dx2_ref[...] + r * (dxh - xh * jnp.mean(dxh * xh, axis=-1, keepdims=True))
        if moves:
            moves.wait(also=(i == nb - 1))

    row = lambda w: pl.BlockSpec((tm, w), lambda i: (i, 0))
    full = lambda a: pl.BlockSpec(a.shape, lambda i: (0, 0))
    ex_in = exchange.operands if exchange else []
    ex_out = exchange.out_shapes if exchange else []
    res = pl.pallas_call(
        body, name="inproj_bwd", grid=(nb,),
        in_specs=[row(SEC), row(SEC), row(GATE_COLS), full(w_a), full(w_b), full(w_g), row(d), row(d), full(g)]
                 + [ANY] * len(ex_in),
        out_specs=[row(d), pl.BlockSpec((1, d), lambda i: (0, 0))] + [ANY] * len(ex_out),
        out_shape=[jax.ShapeDtypeStruct((s, d), F32), jax.ShapeDtypeStruct((1, d), F32)] + ex_out,
        scratch_shapes=exchange.scratch() if exchange else [],
        compiler_params=_params("arbitrary"),
    )(du_a, du_b, du_g, w_a, w_b, w_g, x, dx2, g, *ex_in)
    return res[0], res[1], list(res[2:])


def _rwkv_elementwise(ua, prev_row, first, mu, wl, w0, a0, kkw, kaw, bd):
    tm = ua.shape[0]
    rows = lax.broadcasted_iota(jnp.int32, (tm, 1), 0)
    prev = jnp.where(first, jnp.zeros_like(prev_row), prev_row)
    shifted = jnp.where(rows == 0, prev, pltpu.roll(ua, 1, 0))
    delta = shifted - ua
    us = ua + delta * mu
    r = us[:, 0:512]
    k0 = us[:, 512:1024]
    v = us[:, 1024:1536]
    lo = us[:, 1536:1664]
    gate = us[:, 1664:2176]
    lane = lax.broadcasted_iota(jnp.int32, (1, LANES), 1)
    th = jnp.tanh(lo)
    lin = jnp.where(lane < LORA, th, lo)
    ll = jnp.dot(lin.astype(BF16), wl, preferred_element_type=F32)
    sz = _sigmoid(w0 + ll[:, :512])
    e = sz * math.exp(-0.5)
    dec = jnp.exp(-e)
    a = _sigmoid(a0 + ll[:, 512:])
    kk0 = k0 * kkw
    ss = _head_sum(kk0 * kk0, bd)
    nrm = jnp.maximum(jnp.sqrt(ss), 1e-12)
    kk = kk0 / nrm
    k = k0 * (1.0 + (a - 1.0) * kaw)
    return dict(delta=delta, us=us, r=r, k0=k0, v=v, lo=lo, gate=gate, th=th, lin=lin, sz=sz, e=e, dec=dec,
                a=a, kk0=kk0, ss=ss, nrm=nrm, kk=kk, k=k)


def _rwkv_prep(u_a, mu, wl, w0, a0, kkw, kaw, tm=256):
    s = u_a.shape[0]

    def body(ua_ref, prev_ref, mu_ref, wl_ref, w0_ref, a0_ref, kkw_ref, kaw_ref,
             r_ref, w_ref, k_ref, v_ref, a_ref, b_ref, g_ref):
        i = pl.program_id(0)
        f = _rwkv_elementwise(ua_ref[...], prev_ref[7:8, :], i == 0, mu_ref[...], wl_ref[...], w0_ref[...],
                              a0_ref[...], kkw_ref[...], kaw_ref[...], _head_ones())
        r_ref[...] = f["r"]
        w_ref[...] = f["dec"]
        k_ref[...] = f["k"]
        v_ref[...] = f["v"]
        a_ref[...] = -f["kk"]
        b_ref[...] = f["kk"] * f["a"]
        g_ref[...] = f["gate"]

    vec = lambda w: pl.BlockSpec((1, w), lambda i: (0, 0))
    out = pl.BlockSpec((tm, D_HALF), lambda i: (i, 0))
    return pl.pallas_call(
        body, name="rwkv_prep", grid=(s // tm,),
        in_specs=[pl.BlockSpec((tm, SEC), lambda i: (i, 0)),
                  pl.BlockSpec((8, SEC), lambda i: (jnp.maximum(i * (tm // 8) - 1, 0), 0)),
                  vec(SEC), pl.BlockSpec((LANES, 2 * D_HALF), lambda i: (0, 0)),
                  vec(D_HALF), vec(D_HALF), vec(D_HALF), vec(D_HALF)],
        out_specs=[out] * 7,
        out_shape=[jax.ShapeDtypeStruct((s, D_HALF), F32)] * 7,
        compiler_params=_params("parallel"),
    )(u_a, u_a, mu, wl, w0, a0, kkw, kaw)


SCAN_TB = 128
N_PAIRS = 4


def _pair_sum(x, left):
    s_l = jnp.sum(jnp.where(left, x, 0.0), axis=1, keepdims=True)
    s_r = jnp.sum(jnp.where(left, 0.0, x), axis=1, keepdims=True)
    return jnp.where(left, s_l, s_r)


def _pair_dot(x, row_l, row_r, left):
    s_l = jnp.sum(x * row_l, axis=1, keepdims=True)
    s_r = jnp.sum(x * row_r, axis=1, keepdims=True)
    return jnp.where(left, s_l, s_r)


def _halves(rows8):
    lane = lax.broadcasted_iota(jnp.int32, rows8.shape, 1)
    keep_left = (lane & (LANES - 1)) < HEAD
    return jnp.where(keep_left, rows8, 0.0), jnp.where(keep_left, 0.0, rows8)


def _quad_consts():
    lane = lax.broadcasted_iota(jnp.int32, (HEAD, 2 * LANES), 1)
    rowi = lax.broadcasted_iota(jnp.int32, (HEAD, 2 * LANES), 0)
    diag2 = rowi == (lane & (HEAD - 1))
    r = lax.broadcasted_iota(jnp.int32, (2 * LANES, 2 * LANES), 0) >> 6
    c = lax.broadcasted_iota(jnp.int32, (2 * LANES, 2 * LANES), 1) >> 6
    return diag2, (r == c).astype(BF16)


def _rows_to_columns(x8, diag2, bd2):
    lhs = jnp.concatenate([jnp.where(diag2, x8[i:i + 1], 0.0).astype(BF16) for i in range(SUBLANES)], axis=0)
    return jnp.dot(lhs, bd2, preferred_element_type=F32)


def _diag_rows(qtile, diag2, bd2, sub_row2):
    res = jnp.dot(qtile, bd2, preferred_element_type=F32)
    out = jnp.zeros((SUBLANES, 2 * LANES), F32)
    for i in range(SUBLANES):
        out = jnp.where(sub_row2 == i, _colsum(jnp.where(diag2, res[i * HEAD:(i + 1) * HEAD], 0.0)), out)
    return out


def _store_tile(qbuf, slot, p, i, x):
    qbuf[slot, p // 2, i * HEAD:(i + 1) * HEAD, (p % 2) * LANES:(p % 2 + 1) * LANES] = x.astype(BF16)


def _left_half():
    return lax.broadcasted_iota(jnp.int32, (HEAD, LANES), 1) < HEAD


def _split_refs(refs, n_rows, n_out, exchange):
    n_in = len(exchange.operands) if exchange else 0
    n_ex_out = len(exchange.out_shapes) if exchange else 0
    refs = list(refs)
    rows, refs = refs[:n_rows], refs[n_rows:]
    ex_in, refs = refs[:n_in], refs[n_in:]
    outs, refs = refs[:n_out], refs[n_out:]
    ex_out, refs = refs[:n_ex_out], refs[n_ex_out:]
    scratch, sems = (refs[:-3], refs[-3:]) if exchange else (refs, None)
    moves = exchange.moves(ex_in, ex_out, sems) if exchange else None
    return rows, outs, scratch, moves


def _wkv_fwd(r, w, k, a, b, v, exchange=None):
    s = r.shape[0]
    tb = SCAN_TB
    nb = s // tb

    def body(*refs):
        (r_ref, w_ref, k_ref, a_ref, b_ref, v_ref), (y_ref, st_ref), (state, vbuf, qbuf), moves = _split_refs(
            refs, 6, 2, exchange)
        g = pl.program_id(0)
        if moves:
            moves.start(also=(g == 0))

        @pl.when(g == 0)
        def _():
            state[...] = jnp.zeros_like(state)
            qbuf[...] = jnp.zeros_like(qbuf)

        left = _left_half()
        diag2, bd2 = _quad_consts()
        sub_row2 = lax.broadcasted_iota(jnp.int32, (SUBLANES, 2 * LANES), 0)
        groups = tb // SUBLANES
        quads = [slice(g2 * 2 * LANES, (g2 + 1) * 2 * LANES) for g2 in range(2)]

        def rows_of(q):
            return pl.ds(pl.multiple_of(q * SUBLANES, SUBLANES), SUBLANES)

        def v_tiles(q, slot):
            v8 = v_ref[rows_of(q), :]
            for g2 in range(2):
                vbuf[slot, g2] = _rows_to_columns(v8[:, quads[g2]], diag2, bd2)

        def chain(q, slot):
            rows8 = rows_of(q)
            a8, w8, b8, k8, r8 = (x[rows8, :] for x in (a_ref, w_ref, b_ref, k_ref, r_ref))
            pairs = [slice(p * LANES, (p + 1) * LANES) for p in range(N_PAIRS)]
            a_next = pltpu.roll(a8, SUBLANES - 1, 0)
            (a8_l, a8_r), (wa8_l, wa8_r) = _halves(a8), _halves(w8 * a_next)
            ba8 =jnp.concatenate([_pair_sum(b8[:, pr] * a_next[:, pr], left[0:SUBLANES]) for pr in pairs], axis=1)
            ka8 = jnp.concatenate([_pair_sum(k8[:, pr] * a_next[:, pr], left[0:SUBLANES]) for pr in pairs], axis=1)
            sp = [state[p] for p in range(N_PAIRS)]
            for i in range(0, SUBLANES, 2):
                r0, r1 = slice(i, i + 1), slice(i + 1, i + 2)
                sums = [(_pair_dot(sp[p], a8_l[r0, pairs[p]], a8_r[r0, pairs[p]], left),
                         _pair_dot(sp[p], wa8_l[r0, pairs[p]], wa8_r[r0, pairs[p]], left)) for p in range(N_PAIRS)]
                sa0, sa1 = [s[0] for s in sums], [s[1] for s in sums]
                for p in range(N_PAIRS):
                    pr = pairs[p]
                    inner = slice((p % 2) * LANES, (p % 2 + 1) * LANES)
                    vt0 = vbuf[slot, p // 2, i * HEAD:(i + 1) * HEAD, inner]
                    vt1 = vbuf[slot, p // 2, (i + 1) * HEAD:(i + 2) * HEAD, inner]
                    sa_next = sa1[p] + sa0[p] * ba8[r0, pr] + vt0 * ka8[r0, pr]
                    s1 = sp[p] * w8[r0, pr] + sa0[p] * b8[r0, pr] + vt0 * k8[r0, pr]
                    st_ref[q * SUBLANES + i, p] = s1
                    _store_tile(qbuf, slot, p, i, s1 * r8[r0, pr])
                    s2 = s1 * w8[r1, pr] + sa_next * b8[r1, pr] + vt1 * k8[r1, pr]
                    st_ref[q * SUBLANES + i + 1, p] = s2
                    _store_tile(qbuf, slot, p, i + 1, s2 * r8[r1, pr])
                    sp[p] = s2
            for p in range(N_PAIRS):
                state[p] = sp[p]

        def y_rows(q, slot):
            for g2 in range(2):
                y_ref[rows_of(q), quads[g2]] = _diag_rows(qbuf[slot, g2], diag2, bd2, sub_row2)

        v_tiles(0, 0)

        def two_groups(j, carry):
            q0 = 2 * j
            v_tiles(q0 + 1, 1)
            chain(q0, 0)
            y_rows(jnp.maximum(q0 - 1, 0), 1)
            v_tiles(jnp.minimum(q0 + 2, groups - 1), 0)
            chain(q0 + 1, 1)
            y_rows(q0, 0)
            return carry

        lax.fori_loop(0, groups // 2, two_groups, 0)
        y_rows(groups - 1, 1)
        if moves:
            moves.wait(also=(g == nb - 1))

    rows = pl.BlockSpec((tb, D_HALF), lambda g: (g, 0))
    ex_in = exchange.operands if exchange else []
    ex_out = exchange.out_shapes if exchange else []
    res = pl.pallas_call(
        body, name="wkv_fwd", grid=(nb,),
        in_specs=[rows] * 6 + [ANY] * len(ex_in),
        out_specs=[rows, pl.BlockSpec((tb, N_PAIRS, HEAD, LANES), lambda g: (g, 0, 0, 0))] + [ANY] * len(ex_out),
        out_shape=[jax.ShapeDtypeStruct((s, D_HALF), F32),
                   jax.ShapeDtypeStruct((s, N_PAIRS, HEAD, LANES), F32)] + ex_out,
        scratch_shapes=[pltpu.VMEM((N_PAIRS, HEAD, LANES), F32),
                        pltpu.VMEM((2, 2, SUBLANES * HEAD, 2 * LANES), F32),
                        pltpu.VMEM((2, 2, SUBLANES * HEAD, 2 * LANES), BF16)]
                       + (exchange.scratch() if exchange else []),
        compiler_params=_params("arbitrary"),
    )(r, w, k, a, b, v, *ex_in)
    return res[0], res[1], list(res[2:])


def _wkv_bwd(r, w, k, a, b, v, dy, st, exchange=None):
    s = r.shape[0]
    tb = SCAN_TB
    nb = s // tb

    def body(*refs):
        ((r_ref, w_ref, k_ref, a_ref, b_ref, v_ref, dy_ref, st_ref, before_ref),
         (dr_ref, dw_ref, dk_ref, dv_ref, da_ref, db_ref), (dstate, vbuf, qbuf, sbuf),
         moves) = _split_refs(refs, 9, 6, exchange)
        g = pl.program_id(0)
        first_block = g == nb - 1
        if moves:
            moves.start(also=(g == 0))

        @pl.when(g == 0)
        def _():
            dstate[...] = jnp.zeros_like(dstate)
            qbuf[...] = jnp.zeros_like(qbuf)

        left = _left_half()
        diag2, bd2 = _quad_consts()
        sub_row = lax.broadcasted_iota(jnp.int32, (SUBLANES, LANES), 0)
        sub_row2 = lax.broadcasted_iota(jnp.int32, (SUBLANES, 2 * LANES), 0)
        groups = tb // SUBLANES
        quads = [slice(g2 * 2 * LANES, (g2 + 1) * 2 * LANES) for g2 in range(2)]
        row_refs = (dr_ref, dw_ref, dk_ref, da_ref, db_ref)

        def rows_of(q):
            return pl.ds(pl.multiple_of(q * SUBLANES, SUBLANES), SUBLANES)

        def state_before(q, i, p):
            if i > 0:
                return st_ref[q * SUBLANES + i - 1, p]
            return jnp.where(q == 0, jnp.where(first_block, 0.0, before_ref[0, p]),
                             st_ref[jnp.maximum(q * SUBLANES - 1, 0), p])

        def column_tiles(q, slot):
            rows8 = rows_of(q)
            for kind, ref in enumerate((v_ref, dy_ref)):
                x8 = ref[rows8, :]
                for g2 in range(2):
                    vbuf[slot, kind, g2] = _rows_to_columns(x8[:, quads[g2]], diag2, bd2)
            a8 = a_ref[rows8, :]
            for i in range(SUBLANES):
                for p in range(N_PAIRS):
                    _store_tile(sbuf, 0, p, i, state_before(q, i, p) * a8[i:i + 1, p * LANES:(p + 1) * LANES])
            for g2 in range(2):
                vbuf[slot, 2, g2] = jnp.dot(sbuf[0, g2], bd2, preferred_element_type=F32)

        def chain(q, slot):
            rows8 = rows_of(q)
            a8, w8, b8, k8, r8 = (x[rows8, :] for x in (a_ref, w_ref, b_ref, k_ref, r_ref))
            b8_l, b8_r = _halves(b8)
            dsp = [dstate[p] for p in range(N_PAIRS)]
            outs = [[jnp.zeros((SUBLANES, LANES), F32) for _ in row_refs] for _ in range(N_PAIRS)]
            after = [st_ref[q * SUBLANES + SUBLANES - 1, p] for p in range(N_PAIRS)]
            for i in reversed(range(SUBLANES)):
                row = slice(i, i + 1)
                pl_ = [slice(p * LANES, (p + 1) * LANES) for p in range(N_PAIRS)]
                tile = [(p // 2, slice(i * HEAD, (i + 1) * HEAD), slice((p % 2) * LANES, (p % 2 + 1) * LANES))
                        for p in range(N_PAIRS)]
                sp = [state_before(q, i, p) for p in range(N_PAIRS)]
                dyt = [vbuf[(slot, 1) + tile[p]] for p in range(N_PAIRS)]
                ds = [dsp[p] + dyt[p] * r8[row, pl_[p]] for p in range(N_PAIRS)]
                dsa = [_pair_dot(ds[p], b8_l[row, pl_[p]], b8_r[row, pl_[p]], left) for p in range(N_PAIRS)]
                sa = [vbuf[(slot, 2) + tile[p]] for p in range(N_PAIRS)]
                for p in range(N_PAIRS):
                    ar, wr, br, kr = (x[row, pl_[p]] for x in (a8, w8, b8, k8))
                    vt = vbuf[(slot, 0) + tile[p]]
                    dsp[p] = ds[p] * wr + dsa[p] * ar
                    new = (_colsum(after[p] * dyt[p]), _colsum(ds[p] * sp[p]), _colsum(ds[p] * vt),
                           _colsum(sp[p] * dsa[p]), _colsum(ds[p] * sa[p]))
                    outs[p] = [jnp.where(sub_row == i, n, o) for n, o in zip(new, outs[p])]
                    _store_tile(qbuf, slot, p, i, ds[p] * kr)
                after = sp
            for p in range(N_PAIRS):
                dstate[p] = dsp[p]
                for ref, o in zip(row_refs, outs[p]):
                    ref[rows8, p * LANES:(p + 1) * LANES] = o

        def dv_rows(q, slot):
            for g2 in range(2):
                dv_ref[rows_of(q), quads[g2]] = _diag_rows(qbuf[slot, g2], diag2, bd2, sub_row2)

        column_tiles(groups - 1, 0)

        def two_groups(j, carry):
            q0 = groups - 1 - 2 * j
            column_tiles(q0 - 1, 1)
            chain(q0, 0)
            dv_rows(jnp.minimum(q0 + 1, groups - 1), 1)
            column_tiles(jnp.maximum(q0 - 2, 0), 0)
            chain(q0 - 1, 1)
            dv_rows(q0, 0)
            return carry

        lax.fori_loop(0, groups // 2, two_groups, 0)
        dv_rows(0, 1)
        if moves:
            moves.wait(also=(g == nb - 1))

    rows = pl.BlockSpec((tb, D_HALF), lambda g: (nb - 1 - g, 0))
    ex_in = exchange.operands if exchange else []
    ex_out = exchange.out_shapes if exchange else []
    res = pl.pallas_call(
        body, name="wkv_bwd", grid=(nb,),
        in_specs=[rows] * 7 + [pl.BlockSpec((tb, N_PAIRS, HEAD, LANES), lambda g: (nb - 1 - g, 0, 0, 0)),
                               pl.BlockSpec((1, N_PAIRS, HEAD, LANES),
                                            lambda g: (jnp.maximum((nb - 1 - g) * tb - 1, 0), 0, 0, 0))]
                 + [ANY] * len(ex_in),
        out_specs=[rows] * 6 + [ANY] * len(ex_out),
        out_shape=[jax.ShapeDtypeStruct((s, D_HALF), F32)] * 6 + ex_out,
        scratch_shapes=[pltpu.VMEM((N_PAIRS, HEAD, LANES), F32),
                        pltpu.VMEM((2, 3, 2, SUBLANES * HEAD, 2 * LANES), F32),
                        pltpu.VMEM((2, 2, SUBLANES * HEAD, 2 * LANES), BF16),
                        pltpu.VMEM((1, 2, SUBLANES * HEAD, 2 * LANES), BF16)]
                       + (exchange.scratch() if exchange else []),
        compiler_params=_params("arbitrary"),
    )(r, w, k, a, b, v, dy, st, st, *ex_in)
    return list(res[:6]), list(res[6:])


def _rwkv_post_math(y, r, k, v, gate, lw, lb, rk, bd):
    mean = _head_sum(y, bd) * (1.0 / HEAD)
    yc = y - mean
    var = _head_sum(yc * yc, bd) * (1.0 / HEAD)
    rstd = lax.rsqrt(var + LNX_EPS)
    yn = yc * rstd
    rkk = _head_sum(r * k * rk, bd)
    sg = _sigmoid(gate)
    pre = yn * lw + lb + rkk * v
    return yn, rstd, rkk, sg, pre


def _rwkv_post(y, r, k, v, gate, lw, lb, rk, tm=256):
    s = y.shape[0]

    def body(y_ref, r_ref, k_ref, v_ref, g_ref, lw_ref, lb_ref, rk_ref, o_ref):
        gate_v = g_ref[...]
        _, _, _, sg, pre = _rwkv_post_math(y_ref[...], r_ref[...], k_ref[...], v_ref[...], gate_v,
                                           lw_ref[...], lb_ref[...], rk_ref[...], _head_ones())
        o_ref[...] = pre * (gate_v * sg)

    blk = pl.BlockSpec((tm, D_HALF), lambda i: (i, 0))
    vec = pl.BlockSpec((1, D_HALF), lambda i: (0, 0))
    return pl.pallas_call(
        body, name="rwkv_post", grid=(s // tm,),
        in_specs=[blk] * 5 + [vec] * 3, out_specs=blk,
        out_shape=jax.ShapeDtypeStruct((s, D_HALF), F32), compiler_params=_params("parallel"),
    )(y, r, k, v, gate, lw, lb, rk)


def _rwkv_post_bwd(dmix, y, r, k, v, gate, lw, lb, rk, tm=256):
    s = y.shape[0]

    def body(dm_ref, y_ref, r_ref, k_ref, v_ref, g_ref, lw_ref, lb_ref, rk_ref,
             dy_ref, dr_ref, dk_ref, dv_ref, dg_ref, dlw_ref, dlb_ref, drk_ref):
        i = pl.program_id(0)

        @pl.when(i == 0)
        def _():
            dlw_ref[...] = jnp.zeros_like(dlw_ref)
            dlb_ref[...] = jnp.zeros_like(dlb_ref)
            drk_ref[...] = jnp.zeros_like(drk_ref)

        bd = _head_ones()
        rv, kv, vv, gate_v, lw_v, rk_v = r_ref[...], k_ref[...], v_ref[...], g_ref[...], lw_ref[...], rk_ref[...]
        yn, rstd, rkk, sg, pre = _rwkv_post_math(y_ref[...], rv, kv, vv, gate_v, lw_v, lb_ref[...], rk_v, bd)
        dm = dm_ref[...]
        dg_ref[...] = dm * pre * (sg * (1.0 + gate_v * (1.0 - sg)))
        dpre = dm * (gate_v * sg)
        dlw_ref[...] += _colsum(dpre * yn)
        dlb_ref[...] += _colsum(dpre)
        dyn = dpre * lw_v
        m1 = _head_sum(dyn, bd) * (1.0 / HEAD)
        m2 = _head_sum(dyn * yn, bd) * (1.0 / HEAD)
        dy_ref[...] = rstd * (dyn - m1 - yn * m2)
        dv_ref[...] = dpre * rkk
        drkk = _head_sum(dpre * vv, bd)
        dr_ref[...] = drkk * kv * rk_v
        dk_ref[...] = drkk * rv * rk_v
        drk_ref[...] += _colsum(drkk * rv * kv)

    blk = pl.BlockSpec((tm, D_HALF), lambda i: (i, 0))
    vec = pl.BlockSpec((1, D_HALF), lambda i: (0, 0))
    return pl.pallas_call(
        body, name="rwkv_post_bwd", grid=(s // tm,),
        in_specs=[blk] * 6 + [vec] * 3, out_specs=[blk] * 5 + [vec] * 3,
        out_shape=[jax.ShapeDtypeStruct((s, D_HALF), F32)] * 5 + [jax.ShapeDtypeStruct((1, D_HALF), F32)] * 3,
        compiler_params=_params("arbitrary"),
    )(dmix, y, r, k, v, gate, lw, lb, rk)


def _rwkv_prep_bwd(u_a, grads, mu, wl, w0, a0, kkw, kaw, tm=256):
    s = u_a.shape[0]
    nb = s // tm

    def body(ua_ref, prev_ref, drs_ref, dws_ref, dks_ref, dvs_ref, das_ref, dbs_ref, drb_ref, dkb_ref, dvb_ref,
             dgt_ref, mu_ref, wl_ref, w0_ref, a0_ref, kkw_ref, kaw_ref,
             du_ref, dmu_ref, dwl_ref, dw0_ref, da0_ref, dkkw_ref, dkaw_ref, carry):
        i = pl.program_id(0)

        @pl.when(i == 0)
        def _():
            carry[...] = jnp.zeros_like(carry)
            for ref in (dmu_ref, dwl_ref, dw0_ref, da0_ref, dkkw_ref, dkaw_ref):
                ref[...] = jnp.zeros_like(ref)

        bd = _head_ones()
        mu_v, wl_v, kkw_v, kaw_v = mu_ref[...], wl_ref[...], kkw_ref[...], kaw_ref[...]
        f = _rwkv_elementwise(ua_ref[...], prev_ref[7:8, :], i == nb - 1, mu_v, wl_v, w0_ref[...],
                              a0_ref[...], kkw_v, kaw_v, bd)
        a, kk, k0 = f["a"], f["kk"], f["k0"]
        dk = dks_ref[...] + dkb_ref[...]
        dbs = dbs_ref[...]
        dkk = dbs * a - das_ref[...]
        da = dbs * kk + dk * k0 * kaw_v
        dk0 = dk * (1.0 + (a - 1.0) * kaw_v)
        dkaw_ref[...] += _colsum(dk * k0 * (a - 1.0))
        inv = 1.0 / f["nrm"]
        proj = _head_sum(dkk * kk, bd)
        dkk0 = jnp.where(f["ss"] > 1e-24, (dkk - kk * proj) * inv, dkk * inv)
        dk0 = dk0 + dkk0 * kkw_v
        dkkw_ref[...] += _colsum(dkk0 * k0)
        dza = da * a * (1.0 - a)
        da0_ref[...] += _colsum(dza)
        dz = -dws_ref[...] * f["dec"] * f["e"] * (1.0 - f["sz"])
        dw0_ref[...] += _colsum(dz)
        dll = jnp.concatenate([dz, dza], axis=1).astype(BF16)
        dwl_ref[...] += _dot_tn(f["lin"].astype(BF16), dll)
        dlin = _dot_nt(dll, wl_v)
        lane = lax.broadcasted_iota(jnp.int32, (1, LANES), 1)
        th = f["th"]
        dlo = jnp.where(lane < LORA, dlin * (1.0 - th * th), dlin)
        dus = jnp.concatenate([drs_ref[...] + drb_ref[...], dk0, dvs_ref[...] + dvb_ref[...], dlo, dgt_ref[...]],
                              axis=1)
        dmu_ref[...] += _colsum(dus * f["delta"])
        g1 = dus * mu_v
        rows = lax.broadcasted_iota(jnp.int32, (tm, 1), 0)
        up = jnp.where(rows == tm - 1, carry[...], pltpu.roll(g1, tm - 1, 0))
        du_ref[...] = dus - g1 + up
        carry[...] = g1[0:1, :]

    rev = lambda w: pl.BlockSpec((tm, w), lambda i: (nb - 1 - i, 0))
    vec = lambda w: pl.BlockSpec((1, w), lambda i: (0, 0))
    wl_spec = pl.BlockSpec((LANES, 2 * D_HALF), lambda i: (0, 0))
    return pl.pallas_call(
        body, name="rwkv_prep_bwd", grid=(nb,),
        in_specs=[rev(SEC), pl.BlockSpec((8, SEC), lambda i: (jnp.maximum((nb - 1 - i) * (tm // 8) - 1, 0), 0))]
                 + [rev(D_HALF)] * 10 + [vec(SEC), wl_spec] + [vec(D_HALF)] * 4,
        out_specs=[rev(SEC), vec(SEC), wl_spec] + [vec(D_HALF)] * 4,
        out_shape=[jax.ShapeDtypeStruct((s, SEC), F32), jax.ShapeDtypeStruct((1, SEC), F32),
                   jax.ShapeDtypeStruct((LANES, 2 * D_HALF), F32)] + [jax.ShapeDtypeStruct((1, D_HALF), F32)] * 4,
        scratch_shapes=[pltpu.VMEM((1, SEC), F32)],
        compiler_params=_params("arbitrary"),
    )(u_a, u_a, *grads, mu, wl, w0, a0, kkw, kaw)


def _tri(tm, lower):
    r = lax.broadcasted_iota(jnp.int32, (tm, tm), 0)
    c = lax.broadcasted_iota(jnp.int32, (tm, tm), 1)
    return ((r >= c) if lower else (r <= c)).astype(BF16)


def _head_rms(x, g, bd):
    rinv = lax.rsqrt(_head_sum(x * x, bd) * (1.0 / HEAD) + RMS_EPS)
    xh = x * rinv
    return xh, rinv, xh * g


def _fox_prep(u_b, fb, qg, kg, tm=256):
    s = u_b.shape[0]

    def body(ub_ref, fb_ref, qg_ref, kg_ref, q_ref, k_ref, v_ref, cc_ref, cr_ref, carry):
        i = pl.program_id(0)

        @pl.when(i == 0)
        def _():
            carry[...] = jnp.zeros_like(carry)

        bd = _head_ones()
        _, _, qn = _head_rms(ub_ref[:, 0:512], qg_ref[...], bd)
        _, _, kn = _head_rms(ub_ref[:, 512:1024], kg_ref[...], bd)
        q_ref[...] = (qn * ATT_SCALE).astype(BF16)
        k_ref[...] = kn.astype(BF16)
        v_ref[...] = ub_ref[:, 1024:1536].astype(BF16)
        lane = lax.broadcasted_iota(jnp.int32, (1, LANES), 1)
        logf = jnp.where(lane < N_HEADS, _log_sigmoid(ub_ref[:, 2048:2176] + fb_ref[...]), 0.0)
        cum = _exact_dot(logf, _tri(tm, True), ones_first=True) + carry[...]
        for h in range(N_HEADS):
            cc_ref[h] = jnp.broadcast_to(cum[:, h:h + 1], (tm, LANES))
        cr_ref[...] = jnp.transpose(cum)[0:N_HEADS, :]
        carry[...] = cum[tm - 1:tm, :]

    blk = pl.BlockSpec((tm, D_HALF), lambda i: (i, 0))
    return pl.pallas_call(
        body, name="fox_prep", grid=(s // tm,),
        in_specs=[pl.BlockSpec((tm, SEC), lambda i: (i, 0)), pl.BlockSpec((1, LANES), lambda i: (0, 0)),
                  pl.BlockSpec((1, D_HALF), lambda i: (0, 0)), pl.BlockSpec((1, D_HALF), lambda i: (0, 0))],
        out_specs=[blk, blk, blk, pl.BlockSpec((N_HEADS, tm, LANES), lambda i: (0, i, 0)),
                   pl.BlockSpec((N_HEADS, tm), lambda i: (0, i))],
        out_shape=[jax.ShapeDtypeStruct((s, D_HALF), BF16)] * 3
                  + [jax.ShapeDtypeStruct((N_HEADS, s, LANES), F32), jax.ShapeDtypeStruct((N_HEADS, s), F32)],
        scratch_shapes=[pltpu.VMEM((1, LANES), F32)],
        compiler_params=_params("arbitrary"),
    )(u_b, fb, qg, kg)


ATT_T = 256


def _attn_fwd(q, k, v, cc, cr, u_b):
    s = q.shape[0]
    t = ATT_T
    nblk = s // t

    def body(q_ref, k_ref, v_ref, cc_ref, cr_ref, g_ref, o_ref, mix_ref, lse_ref, m_sc, l_sc, acc_sc):
        i = pl.program_id(0)
        j = pl.program_id(1)

        @pl.when(j == 0)
        def _():
            m_sc[...] = jnp.full_like(m_sc, NEG)
            l_sc[...] = jnp.zeros_like(l_sc)
            acc_sc[...] = jnp.zeros_like(acc_sc)

        def tile(on_diagonal):
            causal = _causal_tile(t) if on_diagonal else None
            left = lax.broadcasted_iota(jnp.int32, (1, LANES), 1) < HEAD
            for p in range(N_PAIRS):
                lanes = slice(p * LANES, (p + 1) * LANES)
                q2, k2, v2 = q_ref[:, lanes], k_ref[:, lanes], v_ref[:, lanes]
                acc2 = acc_sc[:, lanes]
                for e in range(2):
                    h = 2 * p + e
                    msk = left if e == 0 else jnp.logical_not(left)
                    sc = _dot_nt(jnp.where(msk, q2, jnp.zeros_like(q2)), k2)
                    sc = sc + (_wide(cc_ref[h]) - cr_ref[h:h + 1, :])
                    if on_diagonal:
                        sc = jnp.where(causal, sc, NEG)
                    m_prev = m_sc[h]
                    m_new = jnp.maximum(m_prev, jnp.max(sc, axis=1, keepdims=True))
                    alpha = jnp.exp(m_prev - m_new)
                    pm = jnp.exp(sc - _wide(m_new))
                    l_sc[h] = alpha * l_sc[h] + jnp.sum(pm, axis=1, keepdims=True)
                    m_sc[h] = m_new
                    pv = jnp.dot(pm.astype(BF16), v2, preferred_element_type=F32)
                    acc2 = jnp.where(msk, alpha * acc2 + pv, acc2)
                acc_sc[:, lanes] = acc2

        pl.when(j < i)(functools.partial(tile, False))
        pl.when(j == i)(functools.partial(tile, True))

        @pl.when(j == i)
        def _():
            left = lax.broadcasted_iota(jnp.int32, (1, LANES), 1) < HEAD
            for p in range(N_PAIRS):
                lanes = slice(p * LANES, (p + 1) * LANES)
                inv = jnp.where(left, 1.0 / l_sc[2 * p], 1.0 / l_sc[2 * p + 1])
                o = acc_sc[:, lanes] * inv
                o_ref[:, lanes] = o
                gate = g_ref[:, lanes]
                mix_ref[:, lanes] = o * (gate * _sigmoid(gate))
            for h in range(N_HEADS):
                lse_ref[h] = m_sc[h] + jnp.log(l_sc[h])

    qblk = pl.BlockSpec((t, D_HALF), lambda i, j: (i, 0))
    kblk = pl.BlockSpec((t, D_HALF), lambda i, j: (jnp.minimum(i, j), 0))
    return pl.pallas_call(
        body, name="fox_attn_fwd", grid=(nblk, nblk),
        in_specs=[qblk, kblk, kblk, pl.BlockSpec((N_HEADS, t, LANES), lambda i, j: (0, i, 0)),
                  pl.BlockSpec((N_HEADS, t), lambda i, j: (0, jnp.minimum(i, j))),
                  pl.BlockSpec((t, D_HALF), lambda i, j: (i, 3))],
        out_specs=[qblk, qblk, pl.BlockSpec((N_HEADS, t, LANES), lambda i, j: (0, i, 0))],
        out_shape=[jax.ShapeDtypeStruct((s, D_HALF), F32), jax.ShapeDtypeStruct((s, D_HALF), F32),
                   jax.ShapeDtypeStruct((N_HEADS, s, LANES), F32)],
        scratch_shapes=[pltpu.VMEM((N_HEADS, t, LANES), F32), pltpu.VMEM((N_HEADS, t, LANES), F32),
                        pltpu.VMEM((t, D_HALF), F32)],
        compiler_params=_params("parallel", "arbitrary"),
    )(q, k, v, cc, cr, u_b)


def _fox_post_bwd(dmix, o, u_b, tm=256):
    s = o.shape[0]

    def body(dm_ref, o_ref, g_ref, do_ref, dg_ref):
        gate = g_ref[...]
        sg = _sigmoid(gate)
        dm = dm_ref[...]
        do_ref[...] = (dm * (gate * sg)).astype(BF16)
        dg_ref[...] = dm * o_ref[...] * (sg * (1.0 + gate * (1.0 - sg)))

    blk = pl.BlockSpec((tm, D_HALF), lambda i: (i, 0))
    return pl.pallas_call(
        body, name="fox_post_bwd", grid=(s // tm,),
        in_specs=[blk, blk, pl.BlockSpec((tm, D_HALF), lambda i: (i, 3))], out_specs=[blk] * 2,
        out_shape=[jax.ShapeDtypeStruct((s, D_HALF), BF16), jax.ShapeDtypeStruct((s, D_HALF), F32)],
        compiler_params=_params("parallel"),
    )(dmix, o, u_b)


def _causal_tile(t):
    return lax.broadcasted_iota(jnp.int32, (t, t), 0) >= lax.broadcasted_iota(jnp.int32, (t, t), 1)


def _wide(x):
    return jnp.concatenate([x, x], axis=1)


def _attn_probs(q2, k2, v2, do2, msk, causal, bias, lse_rows):
    zero = jnp.zeros_like(q2)
    qh = jnp.where(msk, q2, zero)
    doh = jnp.where(msk, do2, zero)
    sc = _dot_nt(qh, k2) + bias
    if causal is not None:
        sc = jnp.where(causal, sc, NEG)
    pm = jnp.exp(sc - _wide(lse_rows))
    dp = _dot_nt(doh, v2)
    return qh, doh, pm, dp


def _attn_bwd_rowdot(q, k, v, do, lse, cc, cr):
    s = q.shape[0]
    t = ATT_T
    nblk = s // t

    def body(q_ref, k_ref, v_ref, do_ref, lse_ref, cc_ref, cr_ref, dd_ref, acc):
        i = pl.program_id(0)
        j = pl.program_id(1)

        @pl.when(j == 0)
        def _():
            acc[...] = jnp.zeros_like(acc)

        def tile(on_diagonal):
            causal = _causal_tile(t) if on_diagonal else None
            left = lax.broadcasted_iota(jnp.int32, (1, LANES), 1) < HEAD
            for p in range(N_PAIRS):
                lanes = slice(p * LANES, (p + 1) * LANES)
                q2, k2, v2, do2 = q_ref[:, lanes], k_ref[:, lanes], v_ref[:, lanes], do_ref[:, lanes]
                for e in range(2):
                    h = 2 * p + e
                    msk = left if e == 0 else jnp.logical_not(left)
                    bias = _wide(cc_ref[h]) - cr_ref[h:h + 1, :]
                    _, _, pm, dp = _attn_probs(q2, k2, v2, do2, msk, causal, bias, lse_ref[h])
                    acc[h] += jnp.sum(pm * dp, axis=1, keepdims=True)

        pl.when(j < i)(functools.partial(tile, False))
        pl.when(j == i)(functools.partial(tile, True))

        @pl.when(j == i)
        def _():
            dd_ref[...] = acc[...]

    qblk = pl.BlockSpec((t, D_HALF), lambda i, j: (i, 0))
    qcol = pl.BlockSpec((N_HEADS, t, LANES), lambda i, j: (0, i, 0))
    kblk = pl.BlockSpec((t, D_HALF), lambda i, j: (jnp.minimum(i, j), 0))
    return pl.pallas_call(
        body, name="fox_attn_rowdot", grid=(nblk, nblk),
        in_specs=[qblk, kblk, kblk, qblk, qcol, qcol, pl.BlockSpec((N_HEADS, t), lambda i, j: (0, jnp.minimum(i, j)))],
        out_specs=qcol, out_shape=jax.ShapeDtypeStruct((N_HEADS, s, LANES), F32),
        scratch_shapes=[pltpu.VMEM((N_HEADS, t, LANES), F32)],
        compiler_params=_params("parallel", "arbitrary"),
    )(q, k, v, do, lse, cc, cr)


def _attn_bwd(q, k, v, do, lse, dd, cc, cr):
    s = q.shape[0]
    t = ATT_T
    nblk = s // t

    def body(q_ref, k_ref, v_ref, do_ref, lse_ref, dd_ref, cc_ref, cr_ref,
             dq_ref, dk_ref, dv_ref, dcr_ref, dk_sc, dv_sc, dcr_sc):
        j = pl.program_id(0)
        i = pl.program_id(1)

        @pl.when(jnp.logical_and(j == 0, i == 0))
        def _():
            dq_ref[...] = jnp.zeros_like(dq_ref)

        @pl.when(i == 0)
        def _():
            dk_sc[...] = jnp.zeros_like(dk_sc)
            dv_sc[...] = jnp.zeros_like(dv_sc)
            dcr_sc[...] = jnp.zeros_like(dcr_sc)

        def tile(on_diagonal):
            causal = _causal_tile(t) if on_diagonal else None
            left = lax.broadcasted_iota(jnp.int32, (1, LANES), 1) < HEAD
            qrows = pl.ds(pl.multiple_of(i * t, t), t)
            for p in range(N_PAIRS):
                lanes = slice(p * LANES, (p + 1) * LANES)
                q2, k2, v2, do2 = q_ref[:, lanes], k_ref[:, lanes], v_ref[:, lanes], do_ref[:, lanes]
                zero = jnp.zeros_like(q2)
                dq2 = jnp.zeros((t, LANES), F32)
                dk2 = jnp.zeros((t, LANES), F32)
                dv2 = jnp.zeros((t, LANES), F32)
                for e in range(2):
                    h = 2 * p + e
                    msk = left if e == 0 else jnp.logical_not(left)
                    bias = _wide(cc_ref[h]) - cr_ref[h:h + 1, :]
                    qh, doh, pm, dp = _attn_probs(q2, k2, v2, do2, msk, causal, bias, lse_ref[h])
                    dsc = pm * (dp - _wide(dd_ref[h]))
                    dsb = dsc.astype(BF16)
                    dv2 += _dot_tn(pm.astype(BF16), doh)
                    dk2 += _dot_tn(dsb, qh)
                    dq2 += jnp.dot(dsb, jnp.where(msk, k2, zero), preferred_element_type=F32)
                    dcr_sc[h:h + 1, :] += -_colsum(dsc)
                dq_ref[qrows, lanes] += dq2 * ATT_SCALE
                dk_sc[:, lanes] += dk2
                dv_sc[:, lanes] += dv2

        pl.when(i > j)(functools.partial(tile, False))
        pl.when(i == j)(functools.partial(tile, True))

        @pl.when(i == nblk - 1)
        def _():
            dk_ref[...] = dk_sc[...]
            dv_ref[...] = dv_sc[...]
            dcr_ref[...] = dcr_sc[...]

    qblk = pl.BlockSpec((t, D_HALF), lambda j, i: (jnp.maximum(i, j), 0))
    qcol = pl.BlockSpec((N_HEADS, t, LANES), lambda j, i: (0, jnp.maximum(i, j), 0))
    kblk = pl.BlockSpec((t, D_HALF), lambda j, i: (j, 0))
    return pl.pallas_call(
        body, name="fox_attn_bwd", grid=(nblk, nblk),
        in_specs=[qblk, kblk, kblk, qblk, qcol, qcol, qcol, pl.BlockSpec((N_HEADS, t), lambda j, i: (0, j))],
        out_specs=[pl.BlockSpec((s, D_HALF), lambda j, i: (0, 0)), kblk, kblk,
                   pl.BlockSpec((N_HEADS, t), lambda j, i: (0, j))],
        out_shape=[jax.ShapeDtypeStruct((s, D_HALF), F32)] * 3 + [jax.ShapeDtypeStruct((N_HEADS, s), F32)],
        scratch_shapes=[pltpu.VMEM((t, D_HALF), F32), pltpu.VMEM((t, D_HALF), F32), pltpu.VMEM((N_HEADS, t), F32)],
        compiler_params=_params("arbitrary", "arbitrary"),
    )(q, k, v, do, lse, dd, cc, cr)


def _fox_prep_bwd(u_b, dq, dk, dv, dgate, dcum, fb, qg, kg, tm=256):
    s = u_b.shape[0]
    nb = s // tm

    def body(ub_ref, dq_ref, dk_ref, dv_ref, dg_ref, dc_ref, fb_ref, qg_ref, kg_ref,
             du_ref, dqg_ref, dkg_ref, dfb_ref, carry):
        i = pl.program_id(0)

        @pl.when(i == 0)
        def _():
            carry[...] = jnp.zeros_like(carry)
            dqg_ref[...] = jnp.zeros_like(dqg_ref)
            dkg_ref[...] = jnp.zeros_like(dkg_ref)
            dfb_ref[...] = jnp.zeros_like(dfb_ref)

        bd = _head_ones()
        for lo, g_ref, d_ref, dgain_ref in ((0, qg_ref, dq_ref, dqg_ref), (512, kg_ref, dk_ref, dkg_ref)):
            gain = g_ref[...]
            xh, rinv, _ = _head_rms(ub_ref[:, lo:lo + 512], gain, bd)
            dn = d_ref[...]
            dgain_ref[...] += _colsum(dn * xh)
            dxh = dn * gain
            du_ref[:, lo:lo + 512] = rinv * (dxh - xh * (_head_sum(dxh * xh, bd) * (1.0 / HEAD)))
        du_ref[:, 1024:1536] = dv_ref[...]
        du_ref[:, 1536:2048] = dg_ref[...]
        lane = lax.broadcasted_iota(jnp.int32, (1, LANES), 1)
        dc = dc_ref[...]
        dlogf = _exact_dot(dc, _tri(tm, False), ones_first=True) + carry[...]
        carry[...] += _colsum(dc)
        fl = ub_ref[:, 2048:2176] + fb_ref[...]
        dfl = jnp.where(lane < N_HEADS, dlogf * (1.0 - _sigmoid(fl)), 0.0)
        du_ref[:, 2048:2176] = dfl
        dfb_ref[...] += _colsum(dfl)

    rev = lambda w: pl.BlockSpec((tm, w), lambda i: (nb - 1 - i, 0))
    vec = lambda w: pl.BlockSpec((1, w), lambda i: (0, 0))
    return pl.pallas_call(
        body, name="fox_prep_bwd", grid=(nb,),
        in_specs=[rev(SEC)] + [rev(D_HALF)] * 4 + [rev(LANES), vec(LANES), vec(D_HALF), vec(D_HALF)],
        out_specs=[rev(SEC), vec(D_HALF), vec(D_HALF), vec(LANES)],
        out_shape=[jax.ShapeDtypeStruct((s, SEC), F32), jax.ShapeDtypeStruct((1, D_HALF), F32),
                   jax.ShapeDtypeStruct((1, D_HALF), F32), jax.ShapeDtypeStruct((1, LANES), F32)],
        scratch_shapes=[pltpu.VMEM((1, LANES), F32)],
        compiler_params=_params("arbitrary"),
    )(u_b, dq, dk, dv, dgate, dcum, fb, qg, kg)


def _merge(mix_a, mix_b, u_g, x, tgt, wa, wb, wo, fg, tm=256):
    s, d = x.shape

    def body(ma_ref, mb_ref, ug_ref, x_ref, t_ref, wa_ref, wb_ref, wo_ref, fg_ref,
             dx2_ref, dma_ref, dmb_ref, dug_ref, dwa_ref, dwb_ref, dwo_ref, dfg_ref, loss_ref):
        i = pl.program_id(0)

        @pl.when(i == 0)
        def _():
            for ref in (dwa_ref, dwb_ref, dwo_ref, dfg_ref, loss_ref):
                ref[...] = jnp.zeros_like(ref)

        wa_v, wb_v, wo_v, fg_v = wa_ref[...], wb_ref[...], wo_ref[...], fg_ref[...]
        ma = ma_ref[...].astype(BF16)
        mb = mb_ref[...].astype(BF16)
        ya = jnp.dot(ma, wa_v, preferred_element_type=F32)
        yb = jnp.dot(mb, wb_v, preferred_element_type=F32)
        sa = _sigmoid(ug_ref[:, 0:d])
        sb = _sigmoid(ug_ref[:, d:2 * d])
        merged = (sa * ya + sb * yb).astype(BF16)
        x2 = x_ref[...] + jnp.dot(merged, wo_v, preferred_element_type=F32)
        r2 = lax.rsqrt(jnp.mean(x2 * x2, axis=-1, keepdims=True) + RMS_EPS)
        x2h = x2 * r2
        err = x2h * fg_v - t_ref[...]
        loss_ref[...] += _colsum(err * err)
        dy = err * (1.0 / d)
        dfg_ref[...] += _colsum(dy * x2h)
        dx2h = dy * fg_v
        dx2 = r2 * (dx2h - x2h * jnp.mean(dx2h * x2h, axis=-1, keepdims=True))
        dx2_ref[...] = dx2
        dx2b = dx2.astype(BF16)
        dmerged = _dot_nt(dx2b, wo_v)
        dwo_ref[...] += _dot_tn(merged, dx2b)
        dya = dmerged * sa
        dyb = dmerged * sb
        dug_ref[:, 0:d] = dya * ya * (1.0 - sa)
        dug_ref[:, d:2 * d] = dyb * yb * (1.0 - sb)
        dyab = dya.astype(BF16)
        dybb = dyb.astype(BF16)
        dma_ref[...] = _dot_nt(dyab, wa_v)
        dmb_ref[...] = _dot_nt(dybb, wb_v)
        dwa_ref[...] += _dot_tn(ma, dyab)
        dwb_ref[...] += _dot_tn(mb, dybb)

    row = lambda w: pl.BlockSpec((tm, w), lambda i: (i, 0))
    full = lambda a: pl.BlockSpec(a.shape, lambda i: (0, 0))
    fshape = lambda a: jax.ShapeDtypeStruct(a.shape, F32)
    return pl.pallas_call(
        body, name="merge_fwd_bwd", grid=(s // tm,),
        in_specs=[row(D_HALF), row(D_HALF), row(GATE_COLS), row(d), row(d), full(wa), full(wb), full(wo), full(fg)],
        out_specs=[row(d), row(D_HALF), row(D_HALF), row(GATE_COLS), full(wa), full(wb), full(wo), full(fg), full(fg)],
        out_shape=[jax.ShapeDtypeStruct((s, d), F32), jax.ShapeDtypeStruct((s, D_HALF), F32),
                   jax.ShapeDtypeStruct((s, D_HALF), F32), jax.ShapeDtypeStruct((s, GATE_COLS), F32),
                   fshape(wa), fshape(wb), fshape(wo), fshape(fg), fshape(fg)],
        compiler_params=_params("arbitrary"),
    )(mix_a, mix_b, u_g, x, tgt, wa, wb, wo, fg)


def _lora_weight(w_up, a_up):
    z = jnp.zeros((LORA, D_HALF), w_up.dtype)
    return jnp.concatenate([jnp.concatenate([w_up, z], axis=1), jnp.concatenate([z, a_up], axis=1)], axis=0)


def _device_grads(x, tgt, p, w_a, w_up, a_up, late_weights, fwd_exchange=None, bwd_exchange=None, tail_exchange=None):
    wl = _lora_weight(w_up, a_up)
    rk = p["r_k"].reshape(1, D_HALF)
    fb = jnp.pad(p["f_bias"], ((0, 0), (0, LANES - N_HEADS)))
    qg = jnp.tile(p["q_norm_g"], (1, N_HEADS))
    kg = jnp.tile(p["k_norm_g"], (1, N_HEADS))
    fg = p["final_norm_g"].reshape(1, D_MODEL)
    mixer = (p["shift_mu"], wl, p["w0"], p["a0"], p["k_k"], p["k_a"])

    h = _rmsnorm_in(x, p["norm_g"])
    u_a = _matmul_nn(h, w_a, "inproj_rwkv")
    r, dec, k, v, av, bv, gate_a = _rwkv_prep(u_a, *mixer)
    y, st, arrived = _wkv_fwd(r, dec, k, av, bv, v, fwd_exchange)
    mix_a = _rwkv_post(y, r, k, v, gate_a, p["lnx_w"], p["lnx_b"], rk)

    w_b, w_g, w_out_a, w_out_b, w_out = late_weights(arrived)
    u_b = _matmul_nn(h, w_b, "inproj_fox")
    u_g = _matmul_nn(h, w_g, "inproj_gate")
    q, kn, vb, cc, cr = _fox_prep(u_b, fb, qg, kg)
    o, mix_b, lse = _attn_fwd(q, kn, vb, cc, cr, u_b)

    dx2, dmix_a, dmix_b, du_g, dwa, dwb, dwo, dfg, loss_vec = _merge(
        mix_a, mix_b, u_g, x, tgt, w_out_a, w_out_b, w_out, fg)

    do, dgate_b = _fox_post_bwd(dmix_b, o, u_b)
    dd = _attn_bwd_rowdot(q, kn, vb, do, lse, cc, cr)
    dq, dk_att, dv_att, dcr = _attn_bwd(q, kn, vb, do, lse, dd, cc, cr)
    dcum = jnp.pad(dcr.T, ((0, 0), (0, LANES - N_HEADS)))
    du_b, dqg, dkg, dfb = _fox_prep_bwd(u_b, dq, dk_att, dv_att, dgate_b, dcum, fb, qg, kg)
    h_t = h.T
    dw_b = _matmul_tn_acc(h_t, du_b, "dw_fox")
    dw_g = _matmul_tn_acc(h_t, du_g, "dw_gate")

    dy, dr_b, dk_b, dv_b, dgate_a, dlw, dlb, drk = _rwkv_post_bwd(
        dmix_a, y, r, k, v, gate_a, p["lnx_w"], p["lnx_b"], rk)
    scan_grads, sent = _wkv_bwd(r, dec, k, av, bv, v, dy, st,
                                bwd_exchange(dw_b, dw_g, dwa, dwb, dwo) if bwd_exchange else None)
    du_a, dmu, dwl, dw0, da0, dkkw, dkaw = _rwkv_prep_bwd(u_a, (*scan_grads, dr_b, dk_b, dv_b, dgate_a), *mixer)
    dw_a = _matmul_tn_acc(h_t, du_a, "dw_rwkv")
    dw_up, da_up = dwl[:LORA, :D_HALF], dwl[LORA:, D_HALF:]
    sent_last = _run_on_sequencer(tail_exchange(dw_a, dw_up, da_up), "scatter_tail", 1) if tail_exchange else []
    grad_x, dnorm_g, _ = _inproj_bwd(du_a, du_b, du_g, w_a, w_b, w_g, x, dx2, p["norm_g"])

    grads = dict(
        norm_g=dnorm_g, w_in=(dw_a, dw_b, dw_g), shift_mu=dmu,
        w_lora_up=dw_up, w0=dw0, a_lora_up=da_up, a0=da0, k_k=dkkw, k_a=dkaw,
        r_k=drk.reshape(1, N_HEADS, HEAD), lnx_w=dlw, lnx_b=dlb, f_bias=dfb[:, :N_HEADS],
        q_norm_g=dqg.reshape(N_HEADS, HEAD).sum(axis=0, keepdims=True),
        k_norm_g=dkg.reshape(N_HEADS, HEAD).sum(axis=0, keepdims=True),
        w_out_a=dwa, w_out_b=dwb, w_out=dwo, final_norm_g=dfg.reshape(D_MODEL))
    return loss_vec, grad_x, grads, sent, sent_last


CHIP_FLIPS = ((1, 0), (0, 1), (1, 1))
ANY = pl.BlockSpec(memory_space=pl.ANY)


def _position():
    return lax.axis_index("x"), lax.axis_index("y"), lax.axis_index("c")


def _flip(v, f):
    return 1 - v if f else v


def _both(a, b):
    if a is None:
        return b
    return a if b is None else jnp.logical_and(a, b)


def _when(cond, fn):
    if cond is None:
        fn()
    else:
        pl.when(cond)(fn)


class _Moves:
    def __init__(self, send_sems, recv_sems, local_sems):
        self.send_sems, self.recv_sems, self.local_sems = send_sems, recv_sems, local_sems
        self.remote, self.local = [], []

    def send(self, src, dst, peer, landing, send_if=None, recv_if=None):
        k = len(self.remote)
        sems = dict(send_sem=self.send_sems.at[k], recv_sem=self.recv_sems.at[k], device_id=peer, device_id_type=MESH)
        out = pltpu.make_async_remote_copy(src_ref=src, dst_ref=dst, **sems)
        arrival = pltpu.make_async_remote_copy(src_ref=src, dst_ref=landing, **sems)
        self.remote.append((out, arrival, send_if, recv_if))

    def copy(self, src, dst, cond=None):
        cp = pltpu.make_async_copy(src, dst, self.local_sems.at[len(self.local)])
        self.local.append((cp, cond))

    def start(self, also=None):
        for cp, cond in self.local:
            _when(_both(also, cond), cp.start)
        for out, _, send_if, _ in self.remote:
            _when(_both(also, send_if), out.start)

    def wait_arrivals(self, also=None):
        for _, arrival, _, recv_if in self.remote:
            _when(_both(also, recv_if), arrival.wait_recv)

    def wait_sent(self, also=None):
        for out, _, send_if, _ in self.remote:
            _when(_both(also, send_if), out.wait_send)
        for cp, cond in self.local:
            _when(_both(also, cond), cp.wait)

    def wait(self, also=None):
        self.wait_arrivals(also)
        self.wait_sent(also)


class _Exchange:
    def __init__(self, operands, out_shapes, n_remote, n_local, build, n_relay=0, relay=None):
        self.operands, self.out_shapes = list(operands), list(out_shapes)
        self.n_remote, self.n_local, self.build = n_remote, n_local, build
        self.n_relay, self.relay = n_relay, relay

    def scratch(self):
        return [pltpu.SemaphoreType.DMA((self.n_remote,)), pltpu.SemaphoreType.DMA((self.n_remote,)),
                pltpu.SemaphoreType.DMA((max(self.n_local, 1),))]

    def moves(self, in_refs, out_refs, sems):
        mv = _Moves(*sems)
        self.build(mv, in_refs, out_refs)
        return mv

    def run_alone(self, name):
        n_in, n_out = len(self.operands), len(self.out_shapes)
        relay_scratch = [pltpu.SemaphoreType.DMA((self.n_relay,))] * 2 if self.relay else []

        def body(*refs):
            ins, outs, sems = refs[:n_in], refs[n_in:n_in + n_out], refs[n_in + n_out:]
            mv = self.moves(ins, outs, sems[:3])
            mv.start()
            mv.wait_arrivals()
            if self.relay:
                passed = _Moves(sems[3], sems[4], None)
                self.relay(passed, ins, outs)
                passed.start()
                passed.wait()
            mv.wait_sent()

        return pl.pallas_call(
            body, name=name, in_specs=[ANY] * n_in, out_specs=[ANY] * n_out, out_shape=self.out_shapes,
            scratch_shapes=self.scratch() + relay_scratch, compiler_params=pltpu.CompilerParams(has_side_effects=True),
        )(*self.operands)


def _run_on_sequencer(exchange, name, collective_id):
    ins = [jax.new_ref(a, memory_space=pltpu.MemorySpace.HBM) for a in exchange.operands]
    outs = [jax.empty_ref(s, memory_space=pltpu.MemorySpace.HBM) for s in exchange.out_shapes]

    def launch(send_sems, recv_sems, local_sems):
        x, y, c = _position()
        barrier = pltpu.get_barrier_semaphore()
        for fx, fy in CHIP_FLIPS:
            pl.semaphore_signal(barrier, inc=1, device_id=(_flip(x, fx), _flip(y, fy), c), device_id_type=MESH)
        pl.semaphore_wait(barrier, len(CHIP_FLIPS))
        moves = exchange.moves(ins, outs, (send_sems, recv_sems, local_sems))
        moves.start()
        moves.wait()

    pl.kernel(launch, mesh=plsc.ScalarSubcoreMesh(axis_name="sequencer", num_cores=1), name=name,
              scratch_types=tuple(exchange.scratch()),
              compiler_params=pltpu.CompilerParams(collective_id=collective_id))()
    return [o[...] for o in outs]


def _is_chip(x, y, chip):
    return jnp.logical_and(x == chip // 2, y == chip % 2)


def _gather_exchange(from_chip, from_all, split=()):
    n1, n2 = len(from_chip), len(from_all)

    def rows_of(t, c):
        half = from_chip[t][1].shape[0] // 2
        return pl.ds(c * half, half)

    def build(mv, ins, outs):
        x, y, c = _position()
        me = 2 * x + y
        for t, (chip, _) in enumerate(from_chip):
            mv.copy(ins[t], outs[t], cond=_is_chip(x, y, chip))
        for t in range(n2):
            mv.copy(ins[n1 + t], outs[n1 + t].at[me])
        for fx, fy in CHIP_FLIPS:
            px, py = _flip(x, fx), _flip(y, fy)
            peer = (px, py, c)
            for t, (chip, _) in enumerate(from_chip):
                part = rows_of(t, c) if t in split else slice(None)
                mv.send(ins[t].at[part], outs[t].at[part], peer, landing=outs[t].at[part],
                        send_if=_is_chip(x, y, chip), recv_if=_is_chip(px, py, chip))
            for t in range(n2):
                mv.send(ins[n1 + t], outs[n1 + t].at[me], peer, landing=outs[n1 + t].at[2 * px + py])

    def relay(mv, ins, outs):
        x, y, c = _position()
        for t in split:
            came = jnp.logical_not(_is_chip(x, y, from_chip[t][0]))
            mv.send(outs[t].at[rows_of(t, c)], outs[t].at[rows_of(t, c)], (x, y, 1 - c),
                    landing=outs[t].at[rows_of(t, 1 - c)], send_if=came, recv_if=came)

    arrays = [a for _, a in from_chip] + list(from_all)
    shapes = [jax.ShapeDtypeStruct(a.shape, a.dtype) for _, a in from_chip]
    shapes += [jax.ShapeDtypeStruct((N_CHIPS,) + a.shape, a.dtype) for a in from_all]
    return _Exchange(arrays, shapes, len(CHIP_FLIPS) * (n1 + n2), n1 + n2, build,
                     n_relay=len(split), relay=relay if split else None)


def _scatter_exchange(to_chip, to_all):
    n1, n2 = len(to_chip), len(to_all)

    def build(mv, ins, outs):
        x, y, c = _position()
        for f, (fx, fy) in enumerate(CHIP_FLIPS):
            px, py = _flip(x, fx), _flip(y, fy)
            peer = (px, py, c)
            for t, (chip, _) in enumerate(to_chip):
                mv.send(ins[t], outs[t].at[f], peer, landing=outs[t].at[f],
                        send_if=_is_chip(px, py, chip), recv_if=_is_chip(x, y, chip))
            for t in range(n2):
                mv.send(ins[n1 + t].at[2 * px + py], outs[n1 + t].at[f], peer, landing=outs[n1 + t].at[f])

    arrays = [a for _, a in to_chip] + list(to_all)
    shapes = [jax.ShapeDtypeStruct((len(CHIP_FLIPS),) + a.shape, a.dtype) for _, a in to_chip]
    shapes += [jax.ShapeDtypeStruct((len(CHIP_FLIPS),) + a.shape[1:], a.dtype) for a in to_all]
    return _Exchange(arrays, shapes, len(CHIP_FLIPS) * (n1 + n2), 0, build)


def _swap_sibling(tensors):
    n = len(tensors)

    def body(*refs):
        ins, outs = refs[:n], refs[n:2 * n]
        send_sems, recv_sems = refs[2 * n:]
        x, y, c = _position()
        copies = [pltpu.make_async_remote_copy(
            src_ref=ins[t], dst_ref=outs[t], send_sem=send_sems.at[t], recv_sem=recv_sems.at[t],
            device_id=(x, y, 1 - c), device_id_type=MESH) for t in range(n)]
        for cp in copies:
            cp.start()
        for cp in copies:
            cp.wait_recv()
        for cp in copies:
            cp.wait_send()

    return pl.pallas_call(
        body, name="swap_sibling", in_specs=[ANY] * n, out_specs=[ANY] * n,
        out_shape=[jax.ShapeDtypeStruct(a.shape, a.dtype) for a in tensors],
        scratch_shapes=[pltpu.SemaphoreType.DMA((n,)), pltpu.SemaphoreType.DMA((n,))],
        compiler_params=pltpu.CompilerParams(has_side_effects=True),
    )(*tensors)


def _allreduce_small(slab):
    stages = 3

    def body(x_ref, o_ref, buf, send_sems, recv_sems):
        x, y, c = _position()
        peers = ((1 - x, y, c), (x, 1 - y, c), (x, y, 1 - c))
        o_ref[...] = x_ref[...]
        for k, peer in enumerate(peers):
            cp = pltpu.make_async_remote_copy(src_ref=o_ref, dst_ref=buf.at[k], send_sem=send_sems.at[k],
                                              recv_sem=recv_sems.at[k], device_id=peer, device_id_type=MESH)
            cp.start()
            cp.wait()
            o_ref[...] = o_ref[...] + buf[k]

    return pl.pallas_call(
        body, name="allreduce_small",
        in_specs=[pl.BlockSpec(memory_space=pltpu.VMEM)], out_specs=pl.BlockSpec(memory_space=pltpu.VMEM),
        out_shape=jax.ShapeDtypeStruct(slab.shape, slab.dtype),
        scratch_shapes=[pltpu.VMEM((stages,) + slab.shape, slab.dtype),
                        pltpu.SemaphoreType.DMA((stages,)), pltpu.SemaphoreType.DMA((stages,))],
        compiler_params=pltpu.CompilerParams(has_side_effects=True),
    )(slab)


def _row_tile(r):
    return min(r, 256)


def _sum4(stack, recv, me):
    _, r, c = stack.shape
    tr = _row_tile(r)

    def body(me_ref, own_ref, recv_ref, o_ref):
        o_ref[...] = (((own_ref[...] + recv_ref[0].astype(F32)) + recv_ref[1].astype(F32))
                      + recv_ref[2].astype(F32))

    return pl.pallas_call(
        body, name="sum_partials",
        grid_spec=pltpu.PrefetchScalarGridSpec(
            num_scalar_prefetch=1, grid=(r // tr,),
            in_specs=[pl.BlockSpec((None, tr, c), lambda i, me_ref: (me_ref[0], i, 0)),
                      pl.BlockSpec((len(CHIP_FLIPS), tr, c), lambda i, me_ref: (0, i, 0))],
            out_specs=pl.BlockSpec((tr, c), lambda i, me_ref: (i, 0))),
        out_shape=jax.ShapeDtypeStruct((r, c), F32), compiler_params=_params("parallel"),
    )(me, stack, recv)


def _sum_block(own, recv):
    r, c = own.shape
    tr = _row_tile(r)

    def body(own_ref, recv_ref, o_ref):
        o_ref[...] = (((own_ref[...] + recv_ref[0].astype(F32)) + recv_ref[1].astype(F32))
                      + recv_ref[2].astype(F32))

    return pl.pallas_call(
        body, name="sum_block", grid=(r // tr,),
        in_specs=[pl.BlockSpec((tr, c), lambda i: (i, 0)), pl.BlockSpec((len(CHIP_FLIPS), tr, c), lambda i: (0, i, 0))],
        out_specs=pl.BlockSpec((tr, c), lambda i: (i, 0)),
        out_shape=jax.ShapeDtypeStruct((r, c), F32), compiler_params=_params("parallel"),
    )(own, recv)


def _adamw_math(w, g, m, v):
    m = ADAM_B1 * m + (1.0 - ADAM_B1) * g
    v = ADAM_B2 * v + (1.0 - ADAM_B2) * (g * g)
    m_hat = m / (1.0 - ADAM_B1 ** ADAM_STEP)
    v_hat = v / (1.0 - ADAM_B2 ** ADAM_STEP)
    delta = -ADAM_LR * (m_hat / (jnp.sqrt(v_hat) + ADAM_EPS) + ADAM_WD * w)
    return delta, m, v


def _adamw(w, m, v, g_parts, name):
    r, c = w.shape
    tr = _row_tile(r)
    n = len(g_parts)

    def body(*refs):
        w_ref, m_ref, v_ref = refs[:3]
        g_refs = refs[3:3 + n]
        g_out, d_out, m_out, v_out = refs[3 + n:]
        g = g_refs[0][...]
        for ref in g_refs[1:]:
            g = g + ref[...]
        g_out[...] = g
        d_out[...], m_out[...], v_out[...] = _adamw_math(w_ref[...], g, m_ref[...], v_ref[...])

    blk = pl.BlockSpec((tr, c), lambda i: (i, 0))
    return pl.pallas_call(
        body, name=name, grid=(r // tr,), in_specs=[blk] * (3 + n), out_specs=[blk] * 4,
        out_shape=[jax.ShapeDtypeStruct((r, c), F32)] * 4, compiler_params=_params("parallel"),
    )(w, m, v, *g_parts)


def _adamw_small(total, w, m, v):
    sizes = [w[n].size for n in SMALL]
    flat = lambda d: [d[n].reshape(1, -1) for n in SMALL]
    k = len(SMALL)

    def body(*refs):
        total_ref, w_refs, m_refs, v_refs = refs[0], refs[1:1 + k], refs[1 + k:1 + 2 * k], refs[1 + 2 * k:1 + 3 * k]
        outs = refs[1 + 3 * k:]
        for i, size in enumerate(sizes):
            g = total_ref[i:i + 1, 0:size]
            outs[i][...] = g
            outs[k + i][...], outs[2 * k + i][...], outs[3 * k + i][...] = _adamw_math(
                w_refs[i][...], g, m_refs[i][...], v_refs[i][...])

    res = pl.pallas_call(
        body, name="adamw_small", out_shape=[jax.ShapeDtypeStruct((1, size), F32) for size in sizes] * 4,
        compiler_params=_params(),
    )(total, *flat(w), *flat(m), *flat(v))
    return [{n: res[j * k + i].reshape(w[n].shape) for i, n in enumerate(SMALL)} for j in range(4)]


SHARDED = ("w_in", "w_lora_up", "a_lora_up", "w_out_a", "w_out_b", "w_out")
ROW_SHARDED = ("w_out",)
SMALL = ("norm_g", "shift_mu", "w0", "a0", "k_k", "k_a", "r_k", "lnx_w", "lnx_b", "f_bias", "q_norm_g", "k_norm_g",
         "final_norm_g")
WEIGHTS = ("norm_g", "w_in", "shift_mu", "w_lora_up", "w0", "a_lora_up", "a0", "k_k", "k_a", "r_k", "lnx_w", "lnx_b",
           "f_bias", "q_norm_g", "k_norm_g", "w_out_a", "w_out_b", "w_out", "final_norm_g")
SLAB_ROWS = 16
SLAB_COLS = SEC


def _to_slab(named, extra=None):
    rows = [jnp.pad(named[n].reshape(1, -1), ((0, 0), (0, SLAB_COLS - named[n].size))) for n in SMALL]
    if extra is not None:
        rows.append(jnp.pad(extra.reshape(1, -1), ((0, 0), (0, SLAB_COLS - extra.size))))
    rows.append(jnp.zeros((SLAB_ROWS - len(rows), SLAB_COLS), F32))
    return jnp.concatenate(rows, axis=0)


def _by_chip(g, name):
    if name in ROW_SHARDED:
        return g.reshape(N_CHIPS, g.shape[0] // N_CHIPS, g.shape[1])
    r, c = g.shape
    return g.reshape(r, N_CHIPS, c // N_CHIPS).transpose(1, 0, 2)


def _from_chips(stack, name):
    if name in ROW_SHARDED:
        return stack.reshape(-1, stack.shape[2])
    _, r, c = stack.shape
    return stack.transpose(1, 0, 2).reshape(r, N_CHIPS * c)


def kernel(x, norm_g, w_in, shift_mu, w_lora_up, w0, a_lora_up, a0, k_k, k_a, r_k, lnx_w, lnx_b, f_bias, q_norm_g, k_norm_g, w_out_a, w_out_b, w_out, final_norm_g, loss_target, m_norm_g, m_w_in, m_shift_mu, m_w_lora_up, m_w0, m_a_lora_up, m_a0, m_k_k, m_k_a, m_r_k, m_lnx_w, m_lnx_b, m_f_bias, m_q_norm_g, m_k_norm_g, m_w_out_a, m_w_out_b, m_w_out, m_final_norm_g, v_norm_g, v_w_in, v_shift_mu, v_w_lora_up, v_w0, v_a_lora_up, v_a0, v_k_k, v_k_a, v_r_k, v_lnx_w, v_lnx_b, v_f_bias, v_q_norm_g, v_k_norm_g, v_w_out_a, v_w_out_b, v_w_out, v_final_norm_g):
    w = dict(norm_g=norm_g, w_in=w_in, shift_mu=shift_mu, w_lora_up=w_lora_up, w0=w0, a_lora_up=a_lora_up, a0=a0,
             k_k=k_k, k_a=k_a, r_k=r_k, lnx_w=lnx_w, lnx_b=lnx_b, f_bias=f_bias, q_norm_g=q_norm_g,
             k_norm_g=k_norm_g, w_out_a=w_out_a, w_out_b=w_out_b, w_out=w_out, final_norm_g=final_norm_g)
    m = dict(norm_g=m_norm_g, w_in=m_w_in, shift_mu=m_shift_mu, w_lora_up=m_w_lora_up, w0=m_w0,
             a_lora_up=m_a_lora_up, a0=m_a0, k_k=m_k_k, k_a=m_k_a, r_k=m_r_k, lnx_w=m_lnx_w, lnx_b=m_lnx_b,
             f_bias=m_f_bias, q_norm_g=m_q_norm_g, k_norm_g=m_k_norm_g, w_out_a=m_w_out_a, w_out_b=m_w_out_b,
             w_out=m_w_out, final_norm_g=m_final_norm_g)
    v = dict(norm_g=v_norm_g, w_in=v_w_in, shift_mu=v_shift_mu, w_lora_up=v_w_lora_up, w0=v_w0,
             a_lora_up=v_a_lora_up, a0=v_a0, k_k=v_k_k, k_a=v_k_a, r_k=v_r_k, lnx_w=v_lnx_w, lnx_b=v_lnx_b,
             f_bias=v_f_bias, q_norm_g=v_q_norm_g, k_norm_g=v_k_norm_g, w_out_a=v_w_out_a, w_out_b=v_w_out_b,
             w_out=v_w_out, final_norm_g=v_final_norm_g)
    shapes = {n: w[n].shape for n in WEIGHTS}

    shard = {n: w[n][0].astype(BF16) for n in SHARDED}
    late = ("w_out_a", "w_out_b", "w_out")
    loras = ("w_lora_up", "a_lora_up")
    w_in_head, w_in_tail = shard["w_in"][:, :A_TAIL], shard["w_in"][:, A_TAIL:]
    shard0, shard1_head, up_stack, aup_stack = _gather_exchange(
        [(0, shard["w_in"]), (1, w_in_head)], [shard[n] for n in loras], split=(0,)).run_alone("gather_early")
    w_a = jnp.concatenate([shard0, shard1_head], axis=1)

    def late_weights(arrived):
        shard1_tail, shard2, shard3 = arrived[:3]
        w_b = jnp.concatenate([shard1_tail, shard2[:, :B_TAIL], jnp.zeros((D_MODEL, SEC - FOX_REAL), BF16)], axis=1)
        w_g = jnp.concatenate([shard2[:, B_TAIL:], shard3], axis=1)
        return (w_b, w_g, *[_from_chips(s, n) for n, s in zip(late, arrived[3:])])

    own = {}

    def bwd_exchange(dw_b, dw_g, dwa, dwb, dwo):
        own["tail1"] = dw_b[:, :B_HEAD]
        own["block2"] = jnp.concatenate([dw_b[:, B_HEAD:FOX_REAL], dw_g[:, :G_HEAD]], axis=1)
        own["block3"] = dw_g[:, G_HEAD:]
        own.update({n: _by_chip(g, n) for n, g in zip(late, (dwa, dwb, dwo))})
        return _scatter_exchange([(1, own["tail1"].astype(BF16)), (2, own["block2"].astype(BF16)),
                                  (3, own["block3"].astype(BF16))], [own[n].astype(BF16) for n in late])

    def tail_exchange(dw_a, dw_up, da_up):
        own["block0"], own["head1"] = dw_a[:, :SHARD_COLS], dw_a[:, SHARD_COLS:]
        own.update({n: _by_chip(g, n) for n, g in zip(loras, (dw_up, da_up))})
        return _scatter_exchange([(0, own["block0"].astype(BF16)), (1, own["head1"].astype(BF16))],
                                 [own[n].astype(BF16) for n in loras])

    small = {n: w[n] for n in SMALL}
    loss_vec, grad_x, grads, sent, sent_last = _device_grads(
        x[0], loss_target[0], small, w_a, _from_chips(up_stack, "w_lora_up"), _from_chips(aup_stack, "a_lora_up"),
        late_weights, _gather_exchange([(1, w_in_tail), (2, shard["w_in"]), (3, shard["w_in"])], [shard[n] for n in late]),
        bwd_exchange, tail_exchange)

    total = _allreduce_small(_to_slab(grads, extra=loss_vec))
    loss = (0.5 / D_MODEL) * jnp.sum(total[len(SMALL)])
    out_g, out_d, out_m, out_v = _adamw_small(total, w, m, v)

    xpos, ypos, _ = _position()
    me = (2 * xpos + ypos).astype(jnp.int32).reshape(1)
    core_sum = {"w_in": lax.switch(me[0], [
        lambda: _sum_block(own["block0"], sent_last[0]),
        lambda: jnp.concatenate([_sum_block(own["head1"], sent_last[1]), _sum_block(own["tail1"], sent[0])], axis=1),
        lambda: _sum_block(own["block2"], sent[1]),
        lambda: _sum_block(own["block3"], sent[2])])}
    core_sum.update({n: _sum4(own[n], r, me) for n, r in zip(late, sent[3:])})
    core_sum.update({n: _sum4(own[n], r, me) for n, r in zip(loras, sent_last[2:])})
    sibling_sums = _swap_sibling([core_sum[n] for n in SHARDED])
    for n, theirs in zip(SHARDED, sibling_sums):
        g, d, m2, v2 = _adamw(w[n][0], m[n][0], v[n][0], [core_sum[n], theirs], "adamw_" + n)
        out_g[n], out_d[n], out_m[n], out_v[n] = (a.reshape(shapes[n]) for a in (g, d, m2, v2))

    return (loss, grad_x.reshape(x.shape), *[out_g[n] for n in WEIGHTS], *[out_d[n] for n in WEIGHTS],
            *[out_m[n] for n in WEIGHTS], *[out_v[n] for n in WEIGHTS])
```

```python
import functools
import math

import jax
import jax.numpy as jnp
from jax import lax
from jax.experimental import pallas as pl
from jax.experimental.pallas import tpu as pltpu
from jax.experimental.pallas import tpu_sc as plsc

F32 = jnp.float32
BF16 = jnp.bfloat16

D_MODEL = 1024
D_HALF = 512
HEAD = 64
N_HEADS = 8
LORA = 64
RWKV_COLS = 2176
FOX_REAL = 2056
SEC = 2176
GATE_COLS = 2048
IN_COLS = 6280
N_CHIPS = 4
SHARD_COLS = IN_COLS // N_CHIPS
A_TAIL = RWKV_COLS - SHARD_COLS
B_HEAD = SHARD_COLS - A_TAIL
B_TAIL = FOX_REAL - B_HEAD
G_HEAD = SHARD_COLS - B_TAIL
RMS_EPS = 1e-6
LNX_EPS = 64e-5
ATT_SCALE = HEAD ** -0.5
NEG = -1e30

ADAM_LR = 0.001
ADAM_B1 = 0.9
ADAM_B2 = 0.999
ADAM_EPS = 1e-08
ADAM_WD = 0.01
ADAM_STEP = 10

LANES = 128
SUBLANES = 8
VMEM_LIMIT = 56 * 1024 * 1024
MESH = pl.DeviceIdType.MESH


def _params(*sem):
    return pltpu.CompilerParams(dimension_semantics=sem if sem else None, vmem_limit_bytes=VMEM_LIMIT)


def _sigmoid(x):
    return 1.0 / (1.0 + jnp.exp(-x))


def _log_sigmoid(x):
    return jnp.minimum(x, 0.0) - jnp.log(1.0 + jnp.exp(-jnp.abs(x)))


def _head_ones():
    r = lax.broadcasted_iota(jnp.int32, (LANES, LANES), 0) >> 6
    c = lax.broadcasted_iota(jnp.int32, (LANES, LANES), 1) >> 6
    return (r == c).astype(BF16)


def _split3(x):
    hi = x.astype(BF16)
    r1 = x - hi.astype(F32)
    mid = r1.astype(BF16)
    lo = (r1 - mid.astype(F32)).astype(BF16)
    return hi, mid, lo


def _exact_dot(x, ones_bf16, ones_first=False):
    out = None
    for piece in _split3(x):
        if ones_first:
            t = jnp.dot(ones_bf16, piece, preferred_element_type=F32)
        else:
            t = jnp.dot(piece, ones_bf16, preferred_element_type=F32)
        out = t if out is None else out + t
    return out


def _head_sum(x, bd):
    n = x.shape[1] // LANES
    parts = [_exact_dot(x[:, i * LANES:(i + 1) * LANES], bd) for i in range(n)]
    return parts[0] if n == 1 else jnp.concatenate(parts, axis=1)


def _dot_nt(a, b):
    return lax.dot_general(a, b, (((1,), (1,)), ((), ())), preferred_element_type=F32)


def _dot_tn(a, b):
    return lax.dot_general(a, b, (((0,), (0,)), ((), ())), preferred_element_type=F32)


def _colsum(x):
    return jnp.sum(x, axis=0, keepdims=True)


def _rmsnorm_in(x, g, tm=512):
    s, d = x.shape

    def body(x_ref, g_ref, h_ref):
        xv = x_ref[...]
        r = lax.rsqrt(jnp.mean(xv * xv, axis=-1, keepdims=True) + RMS_EPS)
        h_ref[...] = (xv * r * g_ref[...]).astype(BF16)

    return pl.pallas_call(
        body, name="rmsnorm_in", grid=(s // tm,),
        in_specs=[pl.BlockSpec((tm, d), lambda i: (i, 0)), pl.BlockSpec((1, d), lambda i: (0, 0))],
        out_specs=pl.BlockSpec((tm, d), lambda i: (i, 0)),
        out_shape=jax.ShapeDtypeStruct((s, d), BF16), compiler_params=_params("parallel"),
    )(x, g)


def _matmul_nn(a, b, name, tm=512):
    m, k = a.shape
    n = b.shape[1]

    def body(a_ref, b_ref, o_ref):
        o_ref[...] = jnp.dot(a_ref[...], b_ref[...], preferred_element_type=F32)

    return pl.pallas_call(
        body, name=name, grid=(m // tm,),
        in_specs=[pl.BlockSpec((tm, k), lambda i: (i, 0)), pl.BlockSpec((k, n), lambda i: (0, 0))],
        out_specs=pl.BlockSpec((tm, n), lambda i: (i, 0)),
        out_shape=jax.ShapeDtypeStruct((m, n), F32), compiler_params=_params("parallel"),
    )(a, b)


def _matmul_tn_acc(at, b, name, tk=512):
    m, k = at.shape
    n = b.shape[1]

    def body(a_ref, b_ref, o_ref):
        j = pl.program_id(0)

        @pl.when(j == 0)
        def _():
            o_ref[...] = jnp.zeros_like(o_ref)

        o_ref[...] += jnp.dot(a_ref[...], b_ref[...].astype(BF16), preferred_element_type=F32)

    return pl.pallas_call(
        body, name=name, grid=(k // tk,),
        in_specs=[pl.BlockSpec((m, tk), lambda j: (0, j)), pl.BlockSpec((tk, n), lambda j: (j, 0))],
        out_specs=pl.BlockSpec((m, n), lambda j: (0, 0)),
        out_shape=jax.ShapeDtypeStruct((m, n), F32), compiler_params=_params("arbitrary"),
    )(at, b)


def _inproj_bwd(du_a, du_b, du_g, w_a, w_b, w_g, x, dx2, g, exchange=None, tm=256):
    s, d = x.shape
    nb = s // tm

    def body(*refs):
        ((da_ref, db_ref, dg_ref, wa_ref, wb_ref, wg_ref, x_ref, dx2_ref, g_ref), (gx_ref, gg_ref), _,
         moves) = _split_refs(refs, 9, 2, exchange)
        i = pl.program_id(0)
        if moves:
            moves.start(also=(i == 0))

        @pl.when(i == 0)
        def _():
            gg_ref[...] = jnp.zeros_like(gg_ref)

        dh = _dot_nt(da_ref[...].astype(BF16), wa_ref[...])
        dh += _dot_nt(db_ref[...].astype(BF16), wb_ref[...])
        dh += _dot_nt(dg_ref[...].astype(BF16), wg_ref[...])
        xv = x_ref[...]
        r = lax.rsqrt(jnp.mean(xv * xv, axis=-1, keepdims=True) + RMS_EPS)
        xh = xv * r
        gg_ref[...] += _colsum(dh * xh)
        dxh = dh * g_ref[...]
        gx_ref[...] = dx2_ref[...] + r * (dxh - xh * jnp.mean(dxh * xh, axis=-1, keepdims=True))
        if moves:
            moves.wait(also=(i == nb - 1))

    row = lambda w: pl.BlockSpec((tm, w), lambda i: (i, 0))
    full = lambda a: pl.BlockSpec(a.shape, lambda i: (0, 0))
    ex_in = exchange.operands if exchange else []
    ex_out = exchange.out_shapes if exchange else []
    res = pl.pallas_call(
        body, name="inproj_bwd", grid=(nb,),
        in_specs=[row(SEC), row(SEC), row(GATE_COLS), full(w_a), full(w_b), full(w_g), row(d), row(d), full(g)]
                 + [ANY] * len(ex_in),
        out_specs=[row(d), pl.BlockSpec((1, d), lambda i: (0, 0))] + [ANY] * len(ex_out),
        out_shape=[jax.ShapeDtypeStruct((s, d), F32), jax.ShapeDtypeStruct((1, d), F32)] + ex_out,
        scratch_shapes=exchange.scratch() if exchange else [],
        compiler_params=_params("arbitrary"),
    )(du_a, du_b, du_g, w_a, w_b, w_g, x, dx2, g, *ex_in)
    return res[0], res[1], list(res[2:])


def _rwkv_elementwise(ua, prev_row, first, mu, wl, w0, a0, kkw, kaw, bd):
    tm = ua.shape[0]
    rows = lax.broadcasted_iota(jnp.int32, (tm, 1), 0)
    prev = jnp.where(first, jnp.zeros_like(prev_row), prev_row)
    shifted = jnp.where(rows == 0, prev, pltpu.roll(ua, 1, 0))
    delta = shifted - ua
    us = ua + delta * mu
    r = us[:, 0:512]
    k0 = us[:, 512:1024]
    v = us[:, 1024:1536]
    lo = us[:, 1536:1664]
    gate = us[:, 1664:2176]
    lane = lax.broadcasted_iota(jnp.int32, (1, LANES), 1)
    th = jnp.tanh(lo)
    lin = jnp.where(lane < LORA, th, lo)
    ll = jnp.dot(lin.astype(BF16), wl, preferred_element_type=F32)
    sz = _sigmoid(w0 + ll[:, :512])
    e = sz * math.exp(-0.5)
    dec = jnp.exp(-e)
    a = _sigmoid(a0 + ll[:, 512:])
    kk0 = k0 * kkw
    ss = _head_sum(kk0 * kk0, bd)
    nrm = jnp.maximum(jnp.sqrt(ss), 1e-12)
    kk = kk0 / nrm
    k = k0 * (1.0 + (a - 1.0) * kaw)
    return dict(delta=delta, us=us, r=r, k0=k0, v=v, lo=lo, gate=gate, th=th, lin=lin, sz=sz, e=e, dec=dec,
                a=a, kk0=kk0, ss=ss, nrm=nrm, kk=kk, k=k)


def _rwkv_prep(u_a, mu, wl, w0, a0, kkw, kaw, tm=256):
    s = u_a.shape[0]

    def body(ua_ref, prev_ref, mu_ref, wl_ref, w0_ref, a0_ref, kkw_ref, kaw_ref,
             r_ref, w_ref, k_ref, v_ref, a_ref, b_ref, g_ref):
        i = pl.program_id(0)
        f = _rwkv_elementwise(ua_ref[...], prev_ref[7:8, :], i == 0, mu_ref[...], wl_ref[...], w0_ref[...],
                              a0_ref[...], kkw_ref[...], kaw_ref[...], _head_ones())
        r_ref[...] = f["r"]
        w_ref[...] = f["dec"]
        k_ref[...] = f["k"]
        v_ref[...] = f["v"]
        a_ref[...] = -f["kk"]
        b_ref[...] = f["kk"] * f["a"]
        g_ref[...] = f["gate"]

    vec = lambda w: pl.BlockSpec((1, w), lambda i: (0, 0))
    out = pl.BlockSpec((tm, D_HALF), lambda i: (i, 0))
    return pl.pallas_call(
        body, name="rwkv_prep", grid=(s // tm,),
        in_specs=[pl.BlockSpec((tm, SEC), lambda i: (i, 0)),
                  pl.BlockSpec((8, SEC), lambda i: (jnp.maximum(i * (tm // 8) - 1, 0), 0)),
                  vec(SEC), pl.BlockSpec((LANES, 2 * D_HALF), lambda i: (0, 0)),
                  vec(D_HALF), vec(D_HALF), vec(D_HALF), vec(D_HALF)],
        out_specs=[out] * 7,
        out_shape=[jax.ShapeDtypeStruct((s, D_HALF), F32)] * 7,
        compiler_params=_params("parallel"),
    )(u_a, u_a, mu, wl, w0, a0, kkw, kaw)


SCAN_TB = 128
N_PAIRS = 4


def _pair_sum(x, left):
    s_l = jnp.sum(jnp.where(left, x, 0.0), axis=1, keepdims=True)
    s_r = jnp.sum(jnp.where(left, 0.0, x), axis=1, keepdims=True)
    return jnp.where(left, s_l, s_r)


def _pair_dot(x, row_l, row_r, left):
    s_l = jnp.sum(x * row_l, axis=1, keepdims=True)
    s_r = jnp.sum(x * row_r, axis=1, keepdims=True)
    return jnp.where(left, s_l, s_r)


def _halves(rows8):
    lane = lax.broadcasted_iota(jnp.int32, rows8.shape, 1)
    keep_left = (lane & (LANES - 1)) < HEAD
    return jnp.where(keep_left, rows8, 0.0), jnp.where(keep_left, 0.0, rows8)


def _quad_consts():
    lane = lax.broadcasted_iota(jnp.int32, (HEAD, 2 * LANES), 1)
    rowi = lax.broadcasted_iota(jnp.int32, (HEAD, 2 * LANES), 0)
    diag2 = rowi == (lane & (HEAD - 1))
    r = lax.broadcasted_iota(jnp.int32, (2 * LANES, 2 * LANES), 0) >> 6
    c = lax.broadcasted_iota(jnp.int32, (2 * LANES, 2 * LANES), 1) >> 6
    return diag2, (r == c).astype(BF16)


def _rows_to_columns(x8, diag2, bd2):
    lhs = jnp.concatenate([jnp.where(diag2, x8[i:i + 1], 0.0).astype(BF16) for i in range(SUBLANES)], axis=0)
    return jnp.dot(lhs, bd2, preferred_element_type=F32)


def _diag_rows(qtile, diag2, bd2, sub_row2):
    res = jnp.dot(qtile, bd2, preferred_element_type=F32)
    out = jnp.zeros((SUBLANES, 2 * LANES), F32)
    for i in range(SUBLANES):
        out = jnp.where(sub_row2 == i, _colsum(jnp.where(diag2, res[i * HEAD:(i + 1) * HEAD], 0.0)), out)
    return out


def _store_tile(qbuf, slot, p, i, x):
    qbuf[slot, p // 2, i * HEAD:(i + 1) * HEAD, (p % 2) * LANES:(p % 2 + 1) * LANES] = x.astype(BF16)


def _left_half():
    return lax.broadcasted_iota(jnp.int32, (HEAD, LANES), 1) < HEAD


def _split_refs(refs, n_rows, n_out, exchange):
    n_in = len(exchange.operands) if exchange else 0
    n_ex_out = len(exchange.out_shapes) if exchange else 0
    refs = list(refs)
    rows, refs = refs[:n_rows], refs[n_rows:]
    ex_in, refs = refs[:n_in], refs[n_in:]
    outs, refs = refs[:n_out], refs[n_out:]
    ex_out, refs = refs[:n_ex_out], refs[n_ex_out:]
    scratch, sems = (refs[:-3], refs[-3:]) if exchange else (refs, None)
    moves = exchange.moves(ex_in, ex_out, sems) if exchange else None
    return rows, outs, scratch, moves


def _wkv_fwd(r, w, k, a, b, v, exchange=None):
    s = r.shape[0]
    tb = SCAN_TB
    nb = s // tb

    def body(*refs):
        (r_ref, w_ref, k_ref, a_ref, b_ref, v_ref), (y_ref, st_ref), (state, vbuf, qbuf), moves = _split_refs(
            refs, 6, 2, exchange)
        g = pl.program_id(0)
        if moves:
            moves.start(also=(g == 0))

        @pl.when(g == 0)
        def _():
            state[...] = jnp.zeros_like(state)
            qbuf[...] = jnp.zeros_like(qbuf)

        left = _left_half()
        diag2, bd2 = _quad_consts()
        sub_row2 = lax.broadcasted_iota(jnp.int32, (SUBLANES, 2 * LANES), 0)
        groups = tb // SUBLANES
        quads = [slice(g2 * 2 * LANES, (g2 + 1) * 2 * LANES) for g2 in range(2)]

        def rows_of(q):
            return pl.ds(pl.multiple_of(q * SUBLANES, SUBLANES), SUBLANES)

        def v_tiles(q, slot):
            v8 = v_ref[rows_of(q), :]
            for g2 in range(2):
                vbuf[slot, g2] = _rows_to_columns(v8[:, quads[g2]], diag2, bd2)

        def chain(q, slot):
            rows8 = rows_of(q)
            a8, w8, b8, k8, r8 = (x[rows8, :] for x in (a_ref, w_ref, b_ref, k_ref, r_ref))
            pairs = [slice(p * LANES, (p + 1) * LANES) for p in range(N_PAIRS)]
            a_next = pltpu.roll(a8, SUBLANES - 1, 0)
            (a8_l, a8_r), (wa8_l, wa8_r) = _halves(a8), _halves(w8 * a_next)
            ba8 =jnp.concatenate([_pair_sum(b8[:, pr] * a_next[:, pr], left[0:SUBLANES]) for pr in pairs], axis=1)
            ka8 = jnp.concatenate([_pair_sum(k8[:, pr] * a_next[:, pr], left[0:SUBLANES]) for pr in pairs], axis=1)
            sp = [state[p] for p in range(N_PAIRS)]
            for i in range(0, SUBLANES, 2):
                r0, r1 = slice(i, i + 1), slice(i + 1, i + 2)
                sums = [(_pair_dot(sp[p], a8_l[r0, pairs[p]], a8_r[r0, pairs[p]], left),
                         _pair_dot(sp[p], wa8_l[r0, pairs[p]], wa8_r[r0, pairs[p]], left)) for p in range(N_PAIRS)]
                sa0, sa1 = [s[0] for s in sums], [s[1] for s in sums]
                for p in range(N_PAIRS):
                    pr = pairs[p]
                    inner = slice((p % 2) * LANES, (p % 2 + 1) * LANES)
                    vt0 = vbuf[slot, p // 2, i * HEAD:(i + 1) * HEAD, inner]
                    vt1 = vbuf[slot, p // 2, (i + 1) * HEAD:(i + 2) * HEAD, inner]
                    sa_next = sa1[p] + sa0[p] * ba8[r0, pr] + vt0 * ka8[r0, pr]
                    s1 = sp[p] * w8[r0, pr] + sa0[p] * b8[r0, pr] + vt0 * k8[r0, pr]
                    st_ref[q * SUBLANES + i, p] = s1
                    _store_tile(qbuf, slot, p, i, s1 * r8[r0, pr])
                    s2 = s1 * w8[r1, pr] + sa_next * b8[r1, pr] + vt1 * k8[r1, pr]
                    st_ref[q * SUBLANES + i + 1, p] = s2
                    _store_tile(qbuf, slot, p, i + 1, s2 * r8[r1, pr])
                    sp[p] = s2
            for p in range(N_PAIRS):
                state[p] = sp[p]

        def y_rows(q, slot):
            for g2 in range(2):
                y_ref[rows_of(q), quads[g2]] = _diag_rows(qbuf[slot, g2], diag2, bd2, sub_row2)

        v_tiles(0, 0)

        def two_groups(j, carry):
            q0 = 2 * j
            v_tiles(q0 + 1, 1)
            chain(q0, 0)
            y_rows(jnp.maximum(q0 - 1, 0), 1)
            v_tiles(jnp.minimum(q0 + 2, groups - 1), 0)
            chain(q0 + 1, 1)
            y_rows(q0, 0)
            return carry

        lax.fori_loop(0, groups // 2, two_groups, 0)
        y_rows(groups - 1, 1)
        if moves:
            moves.wait(also=(g == nb - 1))

    rows = pl.BlockSpec((tb, D_HALF), lambda g: (g, 0))
    ex_in = exchange.operands if exchange else []
    ex_out = exchange.out_shapes if exchange else []
    res = pl.pallas_call(
        body, name="wkv_fwd", grid=(nb,),
        in_specs=[rows] * 6 + [ANY] * len(ex_in),
        out_specs=[rows, pl.BlockSpec((tb, N_PAIRS, HEAD, LANES), lambda g: (g, 0, 0, 0))] + [ANY] * len(ex_out),
        out_shape=[jax.ShapeDtypeStruct((s, D_HALF), F32),
                   jax.ShapeDtypeStruct((s, N_PAIRS, HEAD, LANES), F32)] + ex_out,
        scratch_shapes=[pltpu.VMEM((N_PAIRS, HEAD, LANES), F32),
                        pltpu.VMEM((2, 2, SUBLANES * HEAD, 2 * LANES), F32),
                        pltpu.VMEM((2, 2, SUBLANES * HEAD, 2 * LANES), BF16)]
                       + (exchange.scratch() if exchange else []),
        compiler_params=_params("arbitrary"),
    )(r, w, k, a, b, v, *ex_in)
    return res[0], res[1], list(res[2:])


def _wkv_bwd(r, w, k, a, b, v, dy, st, exchange=None):
    s = r.shape[0]
    tb = SCAN_TB
    nb = s // tb

    def body(*refs):
        ((r_ref, w_ref, k_ref, a_ref, b_ref, v_ref, dy_ref, st_ref, before_ref),
         (dr_ref, dw_ref, dk_ref, dv_ref, da_ref, db_ref), (dstate, vbuf, qbuf, sbuf),
         moves) = _split_refs(refs, 9, 6, exchange)
        g = pl.program_id(0)
        first_block = g == nb - 1
        if moves:
            moves.start(also=(g == 0))

        @pl.when(g == 0)
        def _():
            dstate[...] = jnp.zeros_like(dstate)
            qbuf[...] = jnp.zeros_like(qbuf)

        left = _left_half()
        diag2, bd2 = _quad_consts()
        sub_row = lax.broadcasted_iota(jnp.int32, (SUBLANES, LANES), 0)
        sub_row2 = lax.broadcasted_iota(jnp.int32, (SUBLANES, 2 * LANES), 0)
        groups = tb // SUBLANES
        quads = [slice(g2 * 2 * LANES, (g2 + 1) * 2 * LANES) for g2 in range(2)]
        row_refs = (dr_ref, dw_ref, dk_ref, da_ref, db_ref)

        def rows_of(q):
            return pl.ds(pl.multiple_of(q * SUBLANES, SUBLANES), SUBLANES)

        def state_before(q, i, p):
            if i > 0:
                return st_ref[q * SUBLANES + i - 1, p]
            return jnp.where(q == 0, jnp.where(first_block, 0.0, before_ref[0, p]),
                             st_ref[jnp.maximum(q * SUBLANES - 1, 0), p])

        def column_tiles(q, slot):
            rows8 = rows_of(q)
            for kind, ref in enumerate((v_ref, dy_ref)):
                x8 = ref[rows8, :]
                for g2 in range(2):
                    vbuf[slot, kind, g2] = _rows_to_columns(x8[:, quads[g2]], diag2, bd2)
            a8 = a_ref[rows8, :]
            for i in range(SUBLANES):
                for p in range(N_PAIRS):
                    _store_tile(sbuf, 0, p, i, state_before(q, i, p) * a8[i:i + 1, p * LANES:(p + 1) * LANES])
            for g2 in range(2):
                vbuf[slot, 2, g2] = jnp.dot(sbuf[0, g2], bd2, preferred_element_type=F32)

        def chain(q, slot):
            rows8 = rows_of(q)
            a8, w8, b8, k8, r8 = (x[rows8, :] for x in (a_ref, w_ref, b_ref, k_ref, r_ref))
            b8_l, b8_r = _halves(b8)
            dsp = [dstate[p] for p in range(N_PAIRS)]
            outs = [[jnp.zeros((SUBLANES, LANES), F32) for _ in row_refs] for _ in range(N_PAIRS)]
            after = [st_ref[q * SUBLANES + SUBLANES - 1, p] for p in range(N_PAIRS)]
            for i in reversed(range(SUBLANES)):
                row = slice(i, i + 1)
                pl_ = [slice(p * LANES, (p + 1) * LANES) for p in range(N_PAIRS)]
                tile = [(p // 2, slice(i * HEAD, (i + 1) * HEAD), slice((p % 2) * LANES, (p % 2 + 1) * LANES))
                        for p in range(N_PAIRS)]
                sp = [state_before(q, i, p) for p in range(N_PAIRS)]
                dyt = [vbuf[(slot, 1) + tile[p]] for p in range(N_PAIRS)]
                ds = [dsp[p] + dyt[p] * r8[row, pl_[p]] for p in range(N_PAIRS)]
                dsa = [_pair_dot(ds[p], b8_l[row, pl_[p]], b8_r[row, pl_[p]], left) for p in range(N_PAIRS)]
                sa = [vbuf[(slot, 2) + tile[p]] for p in range(N_PAIRS)]
                for p in range(N_PAIRS):
                    ar, wr, br, kr = (x[row, pl_[p]] for x in (a8, w8, b8, k8))
                    vt = vbuf[(slot, 0) + tile[p]]
                    dsp[p] = ds[p] * wr + dsa[p] * ar
                    new = (_colsum(after[p] * dyt[p]), _colsum(ds[p] * sp[p]), _colsum(ds[p] * vt),
                           _colsum(sp[p] * dsa[p]), _colsum(ds[p] * sa[p]))
                    outs[p] = [jnp.where(sub_row == i, n, o) for n, o in zip(new, outs[p])]
                    _store_tile(qbuf, slot, p, i, ds[p] * kr)
                after = sp
            for p in range(N_PAIRS):
                dstate[p] = dsp[p]
                for ref, o in zip(row_refs, outs[p]):
                    ref[rows8, p * LANES:(p + 1) * LANES] = o

        def dv_rows(q, slot):
            for g2 in range(2):
                dv_ref[rows_of(q), quads[g2]] = _diag_rows(qbuf[slot, g2], diag2, bd2, sub_row2)

        column_tiles(groups - 1, 0)

        def two_groups(j, carry):
            q0 = groups - 1 - 2 * j
            column_tiles(q0 - 1, 1)
            chain(q0, 0)
            dv_rows(jnp.minimum(q0 + 1, groups - 1), 1)
            column_tiles(jnp.maximum(q0 - 2, 0), 0)
            chain(q0 - 1, 1)
            dv_rows(q0, 0)
            return carry

        lax.fori_loop(0, groups // 2, two_groups, 0)
        dv_rows(0, 1)
        if moves:
            moves.wait(also=(g == nb - 1))

    rows = pl.BlockSpec((tb, D_HALF), lambda g: (nb - 1 - g, 0))
    ex_in = exchange.operands if exchange else []
    ex_out = exchange.out_shapes if exchange else []
    res = pl.pallas_call(
        body, name="wkv_bwd", grid=(nb,),
        in_specs=[rows] * 7 + [pl.BlockSpec((tb, N_PAIRS, HEAD, LANES), lambda g: (nb - 1 - g, 0, 0, 0)),
                               pl.BlockSpec((1, N_PAIRS, HEAD, LANES),
                                            lambda g: (jnp.maximum((nb - 1 - g) * tb - 1, 0), 0, 0, 0))]
                 + [ANY] * len(ex_in),
        out_specs=[rows] * 6 + [ANY] * len(ex_out),
        out_shape=[jax.ShapeDtypeStruct((s, D_HALF), F32)] * 6 + ex_out,
        scratch_shapes=[pltpu.VMEM((N_PAIRS, HEAD, LANES), F32),
                        pltpu.VMEM((2, 3, 2, SUBLANES * HEAD, 2 * LANES), F32),
                        pltpu.VMEM((2, 2, SUBLANES * HEAD, 2 * LANES), BF16),
                        pltpu.VMEM((1, 2, SUBLANES * HEAD, 2 * LANES), BF16)]
                       + (exchange.scratch() if exchange else []),
        compiler_params=_params("arbitrary"),
    )(r, w, k, a, b, v, dy, st, st, *ex_in)
    return list(res[:6]), list(res[6:])


def _rwkv_post_math(y, r, k, v, gate, lw, lb, rk, bd):
    mean = _head_sum(y, bd) * (1.0 / HEAD)
    yc = y - mean
    var = _head_sum(yc * yc, bd) * (1.0 / HEAD)
    rstd = lax.rsqrt(var + LNX_EPS)
    yn = yc * rstd
    rkk = _head_sum(r * k * rk, bd)
    sg = _sigmoid(gate)
    pre = yn * lw + lb + rkk * v
    return yn, rstd, rkk, sg, pre


def _rwkv_post(y, r, k, v, gate, lw, lb, rk, tm=256):
    s = y.shape[0]

    def body(y_ref, r_ref, k_ref, v_ref, g_ref, lw_ref, lb_ref, rk_ref, o_ref):
        gate_v = g_ref[...]
        _, _, _, sg, pre = _rwkv_post_math(y_ref[...], r_ref[...], k_ref[...], v_ref[...], gate_v,
                                           lw_ref[...], lb_ref[...], rk_ref[...], _head_ones())
        o_ref[...] = pre * (gate_v * sg)

    blk = pl.BlockSpec((tm, D_HALF), lambda i: (i, 0))
    vec = pl.BlockSpec((1, D_HALF), lambda i: (0, 0))
    return pl.pallas_call(
        body, name="rwkv_post", grid=(s // tm,),
        in_specs=[blk] * 5 + [vec] * 3, out_specs=blk,
        out_shape=jax.ShapeDtypeStruct((s, D_HALF), F32), compiler_params=_params("parallel"),
    )(y, r, k, v, gate, lw, lb, rk)


def _rwkv_post_bwd(dmix, y, r, k, v, gate, lw, lb, rk, tm=256):
    s = y.shape[0]

    def body(dm_ref, y_ref, r_ref, k_ref, v_ref, g_ref, lw_ref, lb_ref, rk_ref,
             dy_ref, dr_ref, dk_ref, dv_ref, dg_ref, dlw_ref, dlb_ref, drk_ref):
        i = pl.program_id(0)

        @pl.when(i == 0)
        def _():
            dlw_ref[...] = jnp.zeros_like(dlw_ref)
            dlb_ref[...] = jnp.zeros_like(dlb_ref)
            drk_ref[...] = jnp.zeros_like(drk_ref)

        bd = _head_ones()
        rv, kv, vv, gate_v, lw_v, rk_v = r_ref[...], k_ref[...], v_ref[...], g_ref[...], lw_ref[...], rk_ref[...]
        yn, rstd, rkk, sg, pre = _rwkv_post_math(y_ref[...], rv, kv, vv, gate_v, lw_v, lb_ref[...], rk_v, bd)
        dm = dm_ref[...]
        dg_ref[...] = dm * pre * (sg * (1.0 + gate_v * (1.0 - sg)))
        dpre = dm * (gate_v * sg)
        dlw_ref[...] += _colsum(dpre * yn)
        dlb_ref[...] += _colsum(dpre)
        dyn = dpre * lw_v
        m1 = _head_sum(dyn, bd) * (1.0 / HEAD)
        m2 = _head_sum(dyn * yn, bd) * (1.0 / HEAD)
        dy_ref[...] = rstd * (dyn - m1 - yn * m2)
        dv_ref[...] = dpre * rkk
        drkk = _head_sum(dpre * vv, bd)
        dr_ref[...] = drkk * kv * rk_v
        dk_ref[...] = drkk * rv * rk_v
        drk_ref[...] += _colsum(drkk * rv * kv)

    blk = pl.BlockSpec((tm, D_HALF), lambda i: (i, 0))
    vec = pl.BlockSpec((1, D_HALF), lambda i: (0, 0))
    return pl.pallas_call(
        body, name="rwkv_post_bwd", grid=(s // tm,),
        in_specs=[blk] * 6 + [vec] * 3, out_specs=[blk] * 5 + [vec] * 3,
        out_shape=[jax.ShapeDtypeStruct((s, D_HALF), F32)] * 5 + [jax.ShapeDtypeStruct((1, D_HALF), F32)] * 3,
        compiler_params=_params("arbitrary"),
    )(dmix, y, r, k, v, gate, lw, lb, rk)


def _rwkv_prep_bwd(u_a, grads, mu, wl, w0, a0, kkw, kaw, tm=256):
    s = u_a.shape[0]
    nb = s // tm

    def body(ua_ref, prev_ref, drs_ref, dws_ref, dks_ref, dvs_ref, das_ref, dbs_ref, drb_ref, dkb_ref, dvb_ref,
             dgt_ref, mu_ref, wl_ref, w0_ref, a0_ref, kkw_ref, kaw_ref,
             du_ref, dmu_ref, dwl_ref, dw0_ref, da0_ref, dkkw_ref, dkaw_ref, carry):
        i = pl.program_id(0)

        @pl.when(i == 0)
        def _():
            carry[...] = jnp.zeros_like(carry)
            for ref in (dmu_ref, dwl_ref, dw0_ref, da0_ref, dkkw_ref, dkaw_ref):
                ref[...] = jnp.zeros_like(ref)

        bd = _head_ones()
        mu_v, wl_v, kkw_v, kaw_v = mu_ref[...], wl_ref[...], kkw_ref[...], kaw_ref[...]
        f = _rwkv_elementwise(ua_ref[...], prev_ref[7:8, :], i == nb - 1, mu_v, wl_v, w0_ref[...],
                              a0_ref[...], kkw_v, kaw_v, bd)
        a, kk, k0 = f["a"], f["kk"], f["k0"]
        dk = dks_ref[...] + dkb_ref[...]
        dbs = dbs_ref[...]
        dkk = dbs * a - das_ref[...]
        da = dbs * kk + dk * k0 * kaw_v
        dk0 = dk * (1.0 + (a - 1.0) * kaw_v)
        dkaw_ref[...] += _colsum(dk * k0 * (a - 1.0))
        inv = 1.0 / f["nrm"]
        proj = _head_sum(dkk * kk, bd)
        dkk0 = jnp.where(f["ss"] > 1e-24, (dkk - kk * proj) * inv, dkk * inv)
        dk0 = dk0 + dkk0 * kkw_v
        dkkw_ref[...] += _colsum(dkk0 * k0)
        dza = da * a * (1.0 - a)
        da0_ref[...] += _colsum(dza)
        dz = -dws_ref[...] * f["dec"] * f["e"] * (1.0 - f["sz"])
        dw0_ref[...] += _colsum(dz)
        dll = jnp.concatenate([dz, dza], axis=1).astype(BF16)
        dwl_ref[...] += _dot_tn(f["lin"].astype(BF16), dll)
        dlin = _dot_nt(dll, wl_v)
        lane = lax.broadcasted_iota(jnp.int32, (1, LANES), 1)
        th = f["th"]
        dlo = jnp.where(lane < LORA, dlin * (1.0 - th * th), dlin)
        dus = jnp.concatenate([drs_ref[...] + drb_ref[...], dk0, dvs_ref[...] + dvb_ref[...], dlo, dgt_ref[...]],
                              axis=1)
        dmu_ref[...] += _colsum(dus * f["delta"])
        g1 = dus * mu_v
        rows = lax.broadcasted_iota(jnp.int32, (tm, 1), 0)
        up = jnp.where(rows == tm - 1, carry[...], pltpu.roll(g1, tm - 1, 0))
        du_ref[...] = dus - g1 + up
        carry[...] = g1[0:1, :]

    rev = lambda w: pl.BlockSpec((tm, w), lambda i: (nb - 1 - i, 0))
    vec = lambda w: pl.BlockSpec((1, w), lambda i: (0, 0))
    wl_spec = pl.BlockSpec((LANES, 2 * D_HALF), lambda i: (0, 0))
    return pl.pallas_call(
        body, name="rwkv_prep_bwd", grid=(nb,),
        in_specs=[rev(SEC), pl.BlockSpec((8, SEC), lambda i: (jnp.maximum((nb - 1 - i) * (tm // 8) - 1, 0), 0))]
                 + [rev(D_HALF)] * 10 + [vec(SEC), wl_spec] + [vec(D_HALF)] * 4,
        out_specs=[rev(SEC), vec(SEC), wl_spec] + [vec(D_HALF)] * 4,
        out_shape=[jax.ShapeDtypeStruct((s, SEC), F32), jax.ShapeDtypeStruct((1, SEC), F32),
                   jax.ShapeDtypeStruct((LANES, 2 * D_HALF), F32)] + [jax.ShapeDtypeStruct((1, D_HALF), F32)] * 4,
        scratch_shapes=[pltpu.VMEM((1, SEC), F32)],
        compiler_params=_params("arbitrary"),
    )(u_a, u_a, *grads, mu, wl, w0, a0, kkw, kaw)


def _tri(tm, lower):
    r = lax.broadcasted_iota(jnp.int32, (tm, tm), 0)
    c = lax.broadcasted_iota(jnp.int32, (tm, tm), 1)
    return ((r >= c) if lower else (r <= c)).astype(BF16)


def _head_rms(x, g, bd):
    rinv = lax.rsqrt(_head_sum(x * x, bd) * (1.0 / HEAD) + RMS_EPS)
    xh = x * rinv
    return xh, rinv, xh * g


def _fox_prep(u_b, fb, qg, kg, tm=256):
    s = u_b.shape[0]

    def body(ub_ref, fb_ref, qg_ref, kg_ref, q_ref, k_ref, v_ref, cc_ref, cr_ref, carry):
        i = pl.program_id(0)

        @pl.when(i == 0)
        def _():
            carry[...] = jnp.zeros_like(carry)

        bd = _head_ones()
        _, _, qn = _head_rms(ub_ref[:, 0:512], qg_ref[...], bd)
        _, _, kn = _head_rms(ub_ref[:, 512:1024], kg_ref[...], bd)
        q_ref[...] = (qn * ATT_SCALE).astype(BF16)
        k_ref[...] = kn.astype(BF16)
        v_ref[...] = ub_ref[:, 1024:1536].astype(BF16)
        lane = lax.broadcasted_iota(jnp.int32, (1, LANES), 1)
        logf = jnp.where(lane < N_HEADS, _log_sigmoid(ub_ref[:, 2048:2176] + fb_ref[...]), 0.0)
        cum = _exact_dot(logf, _tri(tm, True), ones_first=True) + carry[...]
        for h in range(N_HEADS):
            cc_ref[h] = jnp.broadcast_to(cum[:, h:h + 1], (tm, LANES))
        cr_ref[...] = jnp.transpose(cum)[0:N_HEADS, :]
        carry[...] = cum[tm - 1:tm, :]

    blk = pl.BlockSpec((tm, D_HALF), lambda i: (i, 0))
    return pl.pallas_call(
        body, name="fox_prep", grid=(s // tm,),
        in_specs=[pl.BlockSpec((tm, SEC), lambda i: (i, 0)), pl.BlockSpec((1, LANES), lambda i: (0, 0)),
                  pl.BlockSpec((1, D_HALF), lambda i: (0, 0)), pl.BlockSpec((1, D_HALF), lambda i: (0, 0))],
        out_specs=[blk, blk, blk, pl.BlockSpec((N_HEADS, tm, LANES), lambda i: (0, i, 0)),
                   pl.BlockSpec((N_HEADS, tm), lambda i: (0, i))],
        out_shape=[jax.ShapeDtypeStruct((s, D_HALF), BF16)] * 3
                  + [jax.ShapeDtypeStruct((N_HEADS, s, LANES), F32), jax.ShapeDtypeStruct((N_HEADS, s), F32)],
        scratch_shapes=[pltpu.VMEM((1, LANES), F32)],
        compiler_params=_params("arbitrary"),
    )(u_b, fb, qg, kg)


ATT_T = 256


def _attn_fwd(q, k, v, cc, cr, u_b):
    s = q.shape[0]
    t = ATT_T
    nblk = s // t

    def body(q_ref, k_ref, v_ref, cc_ref, cr_ref, g_ref, o_ref, mix_ref, lse_ref, m_sc, l_sc, acc_sc):
        i = pl.program_id(0)
        j = pl.program_id(1)

        @pl.when(j == 0)
        def _():
            m_sc[...] = jnp.full_like(m_sc, NEG)
            l_sc[...] = jnp.zeros_like(l_sc)
            acc_sc[...] = jnp.zeros_like(acc_sc)

        def tile(on_diagonal):
            causal = _causal_tile(t) if on_diagonal else None
            left = lax.broadcasted_iota(jnp.int32, (1, LANES), 1) < HEAD
            for p in range(N_PAIRS):
                lanes = slice(p * LANES, (p + 1) * LANES)
                q2, k2, v2 = q_ref[:, lanes], k_ref[:, lanes], v_ref[:, lanes]
                acc2 = acc_sc[:, lanes]
                for e in range(2):
                    h = 2 * p + e
                    msk = left if e == 0 else jnp.logical_not(left)
                    sc = _dot_nt(jnp.where(msk, q2, jnp.zeros_like(q2)), k2)
                    sc = sc + (_wide(cc_ref[h]) - cr_ref[h:h + 1, :])
                    if on_diagonal:
                        sc = jnp.where(causal, sc, NEG)
                    m_prev = m_sc[h]
                    m_new = jnp.maximum(m_prev, jnp.max(sc, axis=1, keepdims=True))
                    alpha = jnp.exp(m_prev - m_new)
                    pm = jnp.exp(sc - _wide(m_new))
                    l_sc[h] = alpha * l_sc[h] + jnp.sum(pm, axis=1, keepdims=True)
                    m_sc[h] = m_new
                    pv = jnp.dot(pm.astype(BF16), v2, preferred_element_type=F32)
                    acc2 = jnp.where(msk, alpha * acc2 + pv, acc2)
                acc_sc[:, lanes] = acc2

        pl.when(j < i)(functools.partial(tile, False))
        pl.when(j == i)(functools.partial(tile, True))

        @pl.when(j == i)
        def _():
            left = lax.broadcasted_iota(jnp.int32, (1, LANES), 1) < HEAD
            for p in range(N_PAIRS):
                lanes = slice(p * LANES, (p + 1) * LANES)
                inv = jnp.where(left, 1.0 / l_sc[2 * p], 1.0 / l_sc[2 * p + 1])
                o = acc_sc[:, lanes] * inv
                o_ref[:, lanes] = o
                gate = g_ref[:, lanes]
                mix_ref[:, lanes] = o * (gate * _sigmoid(gate))
            for h in range(N_HEADS):
                lse_ref[h] = m_sc[h] + jnp.log(l_sc[h])

    qblk = pl.BlockSpec((t, D_HALF), lambda i, j: (i, 0))
    kblk = pl.BlockSpec((t, D_HALF), lambda i, j: (jnp.minimum(i, j), 0))
    return pl.pallas_call(
        body, name="fox_attn_fwd", grid=(nblk, nblk),
        in_specs=[qblk, kblk, kblk, pl.BlockSpec((N_HEADS, t, LANES), lambda i, j: (0, i, 0)),
                  pl.BlockSpec((N_HEADS, t), lambda i, j: (0, jnp.minimum(i, j))),
                  pl.BlockSpec((t, D_HALF), lambda i, j: (i, 3))],
        out_specs=[qblk, qblk, pl.BlockSpec((N_HEADS, t, LANES), lambda i, j: (0, i, 0))],
        out_shape=[jax.ShapeDtypeStruct((s, D_HALF), F32), jax.ShapeDtypeStruct((s, D_HALF), F32),
                   jax.ShapeDtypeStruct((N_HEADS, s, LANES), F32)],
        scratch_shapes=[pltpu.VMEM((N_HEADS, t, LANES), F32), pltpu.VMEM((N_HEADS, t, LANES), F32),
                        pltpu.VMEM((t, D_HALF), F32)],
        compiler_params=_params("parallel", "arbitrary"),
    )(q, k, v, cc, cr, u_b)


def _fox_post_bwd(dmix, o, u_b, tm=256):
    s = o.shape[0]

    def body(dm_ref, o_ref, g_ref, do_ref, dg_ref):
        gate = g_ref[...]
        sg = _sigmoid(gate)
        dm = dm_ref[...]
        do_ref[...] = (dm * (gate * sg)).astype(BF16)
        dg_ref[...] = dm * o_ref[...] * (sg * (1.0 + gate * (1.0 - sg)))

    blk = pl.BlockSpec((tm, D_HALF), lambda i: (i, 0))
    return pl.pallas_call(
        body, name="fox_post_bwd", grid=(s // tm,),
        in_specs=[blk, blk, pl.BlockSpec((tm, D_HALF), lambda i: (i, 3))], out_specs=[blk] * 2,
        out_shape=[jax.ShapeDtypeStruct((s, D_HALF), BF16), jax.ShapeDtypeStruct((s, D_HALF), F32)],
        compiler_params=_params("parallel"),
    )(dmix, o, u_b)


def _causal_tile(t):
    return lax.broadcasted_iota(jnp.int32, (t, t), 0) >= lax.broadcasted_iota(jnp.int32, (t, t), 1)


def _wide(x):
    return jnp.concatenate([x, x], axis=1)


def _attn_probs(q2, k2, v2, do2, msk, causal, bias, lse_rows):
    zero = jnp.zeros_like(q2)
    qh = jnp.where(msk, q2, zero)
    doh = jnp.where(msk, do2, zero)
    sc = _dot_nt(qh, k2) + bias
    if causal is not None:
        sc = jnp.where(causal, sc, NEG)
    pm = jnp.exp(sc - _wide(lse_rows))
    dp = _dot_nt(doh, v2)
    return qh, doh, pm, dp


def _attn_bwd_rowdot(q, k, v, do, lse, cc, cr):
    s = q.shape[0]
    t = ATT_T
    nblk = s // t

    def body(q_ref, k_ref, v_ref, do_ref, lse_ref, cc_ref, cr_ref, dd_ref, acc):
        i = pl.program_id(0)
        j = pl.program_id(1)

        @pl.when(j == 0)
        def _():
            acc[...] = jnp.zeros_like(acc)

        def tile(on_diagonal):
            causal = _causal_tile(t) if on_diagonal else None
            left = lax.broadcasted_iota(jnp.int32, (1, LANES), 1) < HEAD
            for p in range(N_PAIRS):
                lanes = slice(p * LANES, (p + 1) * LANES)
                q2, k2, v2, do2 = q_ref[:, lanes], k_ref[:, lanes], v_ref[:, lanes], do_ref[:, lanes]
                for e in range(2):
                    h = 2 * p + e
                    msk = left if e == 0 else jnp.logical_not(left)
                    bias = _wide(cc_ref[h]) - cr_ref[h:h + 1, :]
                    _, _, pm, dp = _attn_probs(q2, k2, v2, do2, msk, causal, bias, lse_ref[h])
                    acc[h] += jnp.sum(pm * dp, axis=1, keepdims=True)

        pl.when(j < i)(functools.partial(tile, False))
        pl.when(j == i)(functools.partial(tile, True))

        @pl.when(j == i)
        def _():
            dd_ref[...] = acc[...]

    qblk = pl.BlockSpec((t, D_HALF), lambda i, j: (i, 0))
    qcol = pl.BlockSpec((N_HEADS, t, LANES), lambda i, j: (0, i, 0))
    kblk = pl.BlockSpec((t, D_HALF), lambda i, j: (jnp.minimum(i, j), 0))
    return pl.pallas_call(
        body, name="fox_attn_rowdot", grid=(nblk, nblk),
        in_specs=[qblk, kblk, kblk, qblk, qcol, qcol, pl.BlockSpec((N_HEADS, t), lambda i, j: (0, jnp.minimum(i, j)))],
        out_specs=qcol, out_shape=jax.ShapeDtypeStruct((N_HEADS, s, LANES), F32),
        scratch_shapes=[pltpu.VMEM((N_HEADS, t, LANES), F32)],
        compiler_params=_params("parallel", "arbitrary"),
    )(q, k, v, do, lse, cc, cr)


def _attn_bwd(q, k, v, do, lse, dd, cc, cr):
    s = q.shape[0]
    t = ATT_T
    nblk = s // t

    def body(q_ref, k_ref, v_ref, do_ref, lse_ref, dd_ref, cc_ref, cr_ref,
             dq_ref, dk_ref, dv_ref, dcr_ref, dk_sc, dv_sc, dcr_sc):
        j = pl.program_id(0)
        i = pl.program_id(1)

        @pl.when(jnp.logical_and(j == 0, i == 0))
        def _():
            dq_ref[...] = jnp.zeros_like(dq_ref)

        @pl.when(i == 0)
        def _():
            dk_sc[...] = jnp.zeros_like(dk_sc)
            dv_sc[...] = jnp.zeros_like(dv_sc)
            dcr_sc[...] = jnp.zeros_like(dcr_sc)

        def tile(on_diagonal):
            causal = _causal_tile(t) if on_diagonal else None
            left = lax.broadcasted_iota(jnp.int32, (1, LANES), 1) < HEAD
            qrows = pl.ds(pl.multiple_of(i * t, t), t)
            for p in range(N_PAIRS):
                lanes = slice(p * LANES, (p + 1) * LANES)
                q2, k2, v2, do2 = q_ref[:, lanes], k_ref[:, lanes], v_ref[:, lanes], do_ref[:, lanes]
                zero = jnp.zeros_like(q2)
                dq2 = jnp.zeros((t, LANES), F32)
                dk2 = jnp.zeros((t, LANES), F32)
                dv2 = jnp.zeros((t, LANES), F32)
                for e in range(2):
                    h = 2 * p + e
                    msk = left if e == 0 else jnp.logical_not(left)
                    bias = _wide(cc_ref[h]) - cr_ref[h:h + 1, :]
                    qh, doh, pm, dp = _attn_probs(q2, k2, v2, do2, msk, causal, bias, lse_ref[h])
                    dsc = pm * (dp - _wide(dd_ref[h]))
                    dsb = dsc.astype(BF16)
                    dv2 += _dot_tn(pm.astype(BF16), doh)
                    dk2 += _dot_tn(dsb, qh)
                    dq2 += jnp.dot(dsb, jnp.where(msk, k2, zero), preferred_element_type=F32)
                    dcr_sc[h:h + 1, :] += -_colsum(dsc)
                dq_ref[qrows, lanes] += dq2 * ATT_SCALE
                dk_sc[:, lanes] += dk2
                dv_sc[:, lanes] += dv2

        pl.when(i > j)(functools.partial(tile, False))
        pl.when(i == j)(functools.partial(tile, True))

        @pl.when(i == nblk - 1)
        def _():
            dk_ref[...] = dk_sc[...]
            dv_ref[...] = dv_sc[...]
            dcr_ref[...] = dcr_sc[...]

    qblk = pl.BlockSpec((t, D_HALF), lambda j, i: (jnp.maximum(i, j), 0))
    qcol = pl.BlockSpec((N_HEADS, t, LANES), lambda j, i: (0, jnp.maximum(i, j), 0))
    kblk = pl.BlockSpec((t, D_HALF), lambda j, i: (j, 0))
    return pl.pallas_call(
        body, name="fox_attn_bwd", grid=(nblk, nblk),
        in_specs=[qblk, kblk, kblk, qblk, qcol, qcol, qcol, pl.BlockSpec((N_HEADS, t), lambda j, i: (0, j))],
        out_specs=[pl.BlockSpec((s, D_HALF), lambda j, i: (0, 0)), kblk, kblk,
                   pl.BlockSpec((N_HEADS, t), lambda j, i: (0, j))],
        out_shape=[jax.ShapeDtypeStruct((s, D_HALF), F32)] * 3 + [jax.ShapeDtypeStruct((N_HEADS, s), F32)],
        scratch_shapes=[pltpu.VMEM((t, D_HALF), F32), pltpu.VMEM((t, D_HALF), F32), pltpu.VMEM((N_HEADS, t), F32)],
        compiler_params=_params("arbitrary", "arbitrary"),
    )(q, k, v, do, lse, dd, cc, cr)


def _fox_prep_bwd(u_b, dq, dk, dv, dgate, dcum, fb, qg, kg, tm=256):
    s = u_b.shape[0]
    nb = s // tm

    def body(ub_ref, dq_ref, dk_ref, dv_ref, dg_ref, dc_ref, fb_ref, qg_ref, kg_ref,
             du_ref, dqg_ref, dkg_ref, dfb_ref, carry):
        i = pl.program_id(0)

        @pl.when(i == 0)
        def _():
            carry[...] = jnp.zeros_like(carry)
            dqg_ref[...] = jnp.zeros_like(dqg_ref)
            dkg_ref[...] = jnp.zeros_like(dkg_ref)
            dfb_ref[...] = jnp.zeros_like(dfb_ref)

        bd = _head_ones()
        for lo, g_ref, d_ref, dgain_ref in ((0, qg_ref, dq_ref, dqg_ref), (512, kg_ref, dk_ref, dkg_ref)):
            gain = g_ref[...]
            xh, rinv, _ = _head_rms(ub_ref[:, lo:lo + 512], gain, bd)
            dn = d_ref[...]
            dgain_ref[...] += _colsum(dn * xh)
            dxh = dn * gain
            du_ref[:, lo:lo + 512] = rinv * (dxh - xh * (_head_sum(dxh * xh, bd) * (1.0 / HEAD)))
        du_ref[:, 1024:1536] = dv_ref[...]
        du_ref[:, 1536:2048] = dg_ref[...]
        lane = lax.broadcasted_iota(jnp.int32, (1, LANES), 1)
        dc = dc_ref[...]
        dlogf = _exact_dot(dc, _tri(tm, False), ones_first=True) + carry[...]
        carry[...] += _colsum(dc)
        fl = ub_ref[:, 2048:2176] + fb_ref[...]
        dfl = jnp.where(lane < N_HEADS, dlogf * (1.0 - _sigmoid(fl)), 0.0)
        du_ref[:, 2048:2176] = dfl
        dfb_ref[...] += _colsum(dfl)

    rev = lambda w: pl.BlockSpec((tm, w), lambda i: (nb - 1 - i, 0))
    vec = lambda w: pl.BlockSpec((1, w), lambda i: (0, 0))
    return pl.pallas_call(
        body, name="fox_prep_bwd", grid=(nb,),
        in_specs=[rev(SEC)] + [rev(D_HALF)] * 4 + [rev(LANES), vec(LANES), vec(D_HALF), vec(D_HALF)],
        out_specs=[rev(SEC), vec(D_HALF), vec(D_HALF), vec(LANES)],
        out_shape=[jax.ShapeDtypeStruct((s, SEC), F32), jax.ShapeDtypeStruct((1, D_HALF), F32),
                   jax.ShapeDtypeStruct((1, D_HALF), F32), jax.ShapeDtypeStruct((1, LANES), F32)],
        scratch_shapes=[pltpu.VMEM((1, LANES), F32)],
        compiler_params=_params("arbitrary"),
    )(u_b, dq, dk, dv, dgate, dcum, fb, qg, kg)


def _merge(mix_a, mix_b, u_g, x, tgt, wa, wb, wo, fg, tm=256):
    s, d = x.shape

    def body(ma_ref, mb_ref, ug_ref, x_ref, t_ref, wa_ref, wb_ref, wo_ref, fg_ref,
             dx2_ref, dma_ref, dmb_ref, dug_ref, dwa_ref, dwb_ref, dwo_ref, dfg_ref, loss_ref):
        i = pl.program_id(0)

        @pl.when(i == 0)
        def _():
            for ref in (dwa_ref, dwb_ref, dwo_ref, dfg_ref, loss_ref):
                ref[...] = jnp.zeros_like(ref)

        wa_v, wb_v, wo_v, fg_v = wa_ref[...], wb_ref[...], wo_ref[...], fg_ref[...]
        ma = ma_ref[...].astype(BF16)
        mb = mb_ref[...].astype(BF16)
        ya = jnp.dot(ma, wa_v, preferred_element_type=F32)
        yb = jnp.dot(mb, wb_v, preferred_element_type=F32)
        sa = _sigmoid(ug_ref[:, 0:d])
        sb = _sigmoid(ug_ref[:, d:2 * d])
        merged = (sa * ya + sb * yb).astype(BF16)
        x2 = x_ref[...] + jnp.dot(merged, wo_v, preferred_element_type=F32)
        r2 = lax.rsqrt(jnp.mean(x2 * x2, axis=-1, keepdims=True) + RMS_EPS)
        x2h = x2 * r2
        err = x2h * fg_v - t_ref[...]
        loss_ref[...] += _colsum(err * err)
        dy = err * (1.0 / d)
        dfg_ref[...] += _colsum(dy * x2h)
        dx2h = dy * fg_v
        dx2 = r2 * (dx2h - x2h * jnp.mean(dx2h * x2h, axis=-1, keepdims=True))
        dx2_ref[...] = dx2
        dx2b = dx2.astype(BF16)
        dmerged = _dot_nt(dx2b, wo_v)
        dwo_ref[...] += _dot_tn(merged, dx2b)
        dya = dmerged * sa
        dyb = dmerged * sb
        dug_ref[:, 0:d] = dya * ya * (1.0 - sa)
        dug_ref[:, d:2 * d] = dyb * yb * (1.0 - sb)
        dyab = dya.astype(BF16)
        dybb = dyb.astype(BF16)
        dma_ref[...] = _dot_nt(dyab, wa_v)
        dmb_ref[...] = _dot_nt(dybb, wb_v)
        dwa_ref[...] += _dot_tn(ma, dyab)
        dwb_ref[...] += _dot_tn(mb, dybb)

    row = lambda w: pl.BlockSpec((tm, w), lambda i: (i, 0))
    full = lambda a: pl.BlockSpec(a.shape, lambda i: (0, 0))
    fshape = lambda a: jax.ShapeDtypeStruct(a.shape, F32)
    return pl.pallas_call(
        body, name="merge_fwd_bwd", grid=(s // tm,),
        in_specs=[row(D_HALF), row(D_HALF), row(GATE_COLS), row(d), row(d), full(wa), full(wb), full(wo), full(fg)],
        out_specs=[row(d), row(D_HALF), row(D_HALF), row(GATE_COLS), full(wa), full(wb), full(wo), full(fg), full(fg)],
        out_shape=[jax.ShapeDtypeStruct((s, d), F32), jax.ShapeDtypeStruct((s, D_HALF), F32),
                   jax.ShapeDtypeStruct((s, D_HALF), F32), jax.ShapeDtypeStruct((s, GATE_COLS), F32),
                   fshape(wa), fshape(wb), fshape(wo), fshape(fg), fshape(fg)],
        compiler_params=_params("arbitrary"),
    )(mix_a, mix_b, u_g, x, tgt, wa, wb, wo, fg)


def _lora_weight(w_up, a_up):
    z = jnp.zeros((LORA, D_HALF), w_up.dtype)
    return jnp.concatenate([jnp.concatenate([w_up, z], axis=1), jnp.concatenate([z, a_up], axis=1)], axis=0)


def _device_grads(x, tgt, p, w_a, w_up, a_up, late_weights, fwd_exchange=None, bwd_exchange=None, tail_exchange=None):
    wl = _lora_weight(w_up, a_up)
    rk = p["r_k"].reshape(1, D_HALF)
    fb = jnp.pad(p["f_bias"], ((0, 0), (0, LANES - N_HEADS)))
    qg = jnp.tile(p["q_norm_g"], (1, N_HEADS))
    kg = jnp.tile(p["k_norm_g"], (1, N_HEADS))
    fg = p["final_norm_g"].reshape(1, D_MODEL)
    mixer = (p["shift_mu"], wl, p["w0"], p["a0"], p["k_k"], p["k_a"])

    h = _rmsnorm_in(x, p["norm_g"])
    u_a = _matmul_nn(h, w_a, "inproj_rwkv")
    r, dec, k, v, av, bv, gate_a = _rwkv_prep(u_a, *mixer)
    y, st, arrived = _wkv_fwd(r, dec, k, av, bv, v, fwd_exchange)
    mix_a = _rwkv_post(y, r, k, v, gate_a, p["lnx_w"], p["lnx_b"], rk)

    w_b, w_g, w_out_a, w_out_b, w_out = late_weights(arrived)
    u_b = _matmul_nn(h, w_b, "inproj_fox")
    u_g = _matmul_nn(h, w_g, "inproj_gate")
    q, kn, vb, cc, cr = _fox_prep(u_b, fb, qg, kg)
    o, mix_b, lse = _attn_fwd(q, kn, vb, cc, cr, u_b)

    dx2, dmix_a, dmix_b, du_g, dwa, dwb, dwo, dfg, loss_vec = _merge(
        mix_a, mix_b, u_g, x, tgt, w_out_a, w_out_b, w_out, fg)

    do, dgate_b = _fox_post_bwd(dmix_b, o, u_b)
    dd = _attn_bwd_rowdot(q, kn, vb, do, lse, cc, cr)
    dq, dk_att, dv_att, dcr = _attn_bwd(q, kn, vb, do, lse, dd, cc, cr)
    dcum = jnp.pad(dcr.T, ((0, 0), (0, LANES - N_HEADS)))
    du_b, dqg, dkg, dfb = _fox_prep_bwd(u_b, dq, dk_att, dv_att, dgate_b, dcum, fb, qg, kg)
    h_t = h.T
    dw_b = _matmul_tn_acc(h_t, du_b, "dw_fox")
    dw_g = _matmul_tn_acc(h_t, du_g, "dw_gate")

    dy, dr_b, dk_b, dv_b, dgate_a, dlw, dlb, drk = _rwkv_post_bwd(
        dmix_a, y, r, k, v, gate_a, p["lnx_w"], p["lnx_b"], rk)
    scan_grads, sent = _wkv_bwd(r, dec, k, av, bv, v, dy, st,
                                bwd_exchange(dw_b, dw_g, dwa, dwb, dwo) if bwd_exchange else None)
    du_a, dmu, dwl, dw0, da0, dkkw, dkaw = _rwkv_prep_bwd(u_a, (*scan_grads, dr_b, dk_b, dv_b, dgate_a), *mixer)
    dw_a = _matmul_tn_acc(h_t, du_a, "dw_rwkv")
    dw_up, da_up = dwl[:LORA, :D_HALF], dwl[LORA:, D_HALF:]
    sent_last = _run_on_sequencer(tail_exchange(dw_a, dw_up, da_up), "scatter_tail", 1) if tail_exchange else []
    grad_x, dnorm_g, _ = _inproj_bwd(du_a, du_b, du_g, w_a, w_b, w_g, x, dx2, p["norm_g"])

    grads = dict(
        norm_g=dnorm_g, w_in=(dw_a, dw_b, dw_g), shift_mu=dmu,
        w_lora_up=dw_up, w0=dw0, a_lora_up=da_up, a0=da0, k_k=dkkw, k_a=dkaw,
        r_k=drk.reshape(1, N_HEADS, HEAD), lnx_w=dlw, lnx_b=dlb, f_bias=dfb[:, :N_HEADS],
        q_norm_g=dqg.reshape(N_HEADS, HEAD).sum(axis=0, keepdims=True),
        k_norm_g=dkg.reshape(N_HEADS, HEAD).sum(axis=0, keepdims=True),
        w_out_a=dwa, w_out_b=dwb, w_out=dwo, final_norm_g=dfg.reshape(D_MODEL))
    return loss_vec, grad_x, grads, sent, sent_last


CHIP_FLIPS = ((1, 0), (0, 1), (1, 1))
ANY = pl.BlockSpec(memory_space=pl.ANY)


def _position():
    return lax.axis_index("x"), lax.axis_index("y"), lax.axis_index("c")


def _flip(v, f):
    return 1 - v if f else v


def _both(a, b):
    if a is None:
        return b
    return a if b is None else jnp.logical_and(a, b)


def _when(cond, fn):
    if cond is None:
        fn()
    else:
        pl.when(cond)(fn)


class _Moves:
    def __init__(self, send_sems, recv_sems, local_sems):
        self.send_sems, self.recv_sems, self.local_sems = send_sems, recv_sems, local_sems
        self.remote, self.local = [], []

    def send(self, src, dst, peer, landing, send_if=None, recv_if=None):
        k = len(self.remote)
        sems = dict(send_sem=self.send_sems.at[k], recv_sem=self.recv_sems.at[k], device_id=peer, device_id_type=MESH)
        out = pltpu.make_async_remote_copy(src_ref=src, dst_ref=dst, **sems)
        arrival = pltpu.make_async_remote_copy(src_ref=src, dst_ref=landing, **sems)
        self.remote.append((out, arrival, send_if, recv_if))

    def copy(self, src, dst, cond=None):
        cp = pltpu.make_async_copy(src, dst, self.local_sems.at[len(self.local)])
        self.local.append((cp, cond))

    def start(self, also=None):
        for cp, cond in self.local:
            _when(_both(also, cond), cp.start)
        for out, _, send_if, _ in self.remote:
            _when(_both(also, send_if), out.start)

    def wait_arrivals(self, also=None):
        for _, arrival, _, recv_if in self.remote:
            _when(_both(also, recv_if), arrival.wait_recv)

    def wait_sent(self, also=None):
        for out, _, send_if, _ in self.remote:
            _when(_both(also, send_if), out.wait_send)
        for cp, cond in self.local:
            _when(_both(also, cond), cp.wait)

    def wait(self, also=None):
        self.wait_arrivals(also)
        self.wait_sent(also)


class _Exchange:
    def __init__(self, operands, out_shapes, n_remote, n_local, build, n_relay=0, relay=None):
        self.operands, self.out_shapes = list(operands), list(out_shapes)
        self.n_remote, self.n_local, self.build = n_remote, n_local, build
        self.n_relay, self.relay = n_relay, relay

    def scratch(self):
        return [pltpu.SemaphoreType.DMA((self.n_remote,)), pltpu.SemaphoreType.DMA((self.n_remote,)),
                pltpu.SemaphoreType.DMA((max(self.n_local, 1),))]

    def moves(self, in_refs, out_refs, sems):
        mv = _Moves(*sems)
        self.build(mv, in_refs, out_refs)
        return mv

    def run_alone(self, name):
        n_in, n_out = len(self.operands), len(self.out_shapes)
        relay_scratch = [pltpu.SemaphoreType.DMA((self.n_relay,))] * 2 if self.relay else []

        def body(*refs):
            ins, outs, sems = refs[:n_in], refs[n_in:n_in + n_out], refs[n_in + n_out:]
            mv = self.moves(ins, outs, sems[:3])
            mv.start()
            mv.wait_arrivals()
            if self.relay:
                passed = _Moves(sems[3], sems[4], None)
                self.relay(passed, ins, outs)
                passed.start()
                passed.wait()
            mv.wait_sent()

        return pl.pallas_call(
            body, name=name, in_specs=[ANY] * n_in, out_specs=[ANY] * n_out, out_shape=self.out_shapes,
            scratch_shapes=self.scratch() + relay_scratch, compiler_params=pltpu.CompilerParams(has_side_effects=True),
        )(*self.operands)


def _run_on_sequencer(exchange, name, collective_id):
    ins = [jax.new_ref(a, memory_space=pltpu.MemorySpace.HBM) for a in exchange.operands]
    outs = [jax.empty_ref(s, memory_space=pltpu.MemorySpace.HBM) for s in exchange.out_shapes]

    def launch(send_sems, recv_sems, local_sems):
        x, y, c = _position()
        barrier = pltpu.get_barrier_semaphore()
        for fx, fy in CHIP_FLIPS:
            pl.semaphore_signal(barrier, inc=1, device_id=(_flip(x, fx), _flip(y, fy), c), device_id_type=MESH)
        pl.semaphore_wait(barrier, len(CHIP_FLIPS))
        moves = exchange.moves(ins, outs, (send_sems, recv_sems, local_sems))
        moves.start()
        moves.wait()

    pl.kernel(launch, mesh=plsc.ScalarSubcoreMesh(axis_name="sequencer", num_cores=1), name=name,
              scratch_types=tuple(exchange.scratch()),
              compiler_params=pltpu.CompilerParams(collective_id=collective_id))()
    return [o[...] for o in outs]


def _is_chip(x, y, chip):
    return jnp.logical_and(x == chip // 2, y == chip % 2)


def _gather_exchange(from_chip, from_all, split=()):
    n1, n2 = len(from_chip), len(from_all)

    def rows_of(t, c):
        half = from_chip[t][1].shape[0] // 2
        return pl.ds(c * half, half)

    def build(mv, ins, outs):
        x, y, c = _position()
        me = 2 * x + y
        for t, (chip, _) in enumerate(from_chip):
            mv.copy(ins[t], outs[t], cond=_is_chip(x, y, chip))
        for t in range(n2):
            mv.copy(ins[n1 + t], outs[n1 + t].at[me])
        for fx, fy in CHIP_FLIPS:
            px, py = _flip(x, fx), _flip(y, fy)
            peer = (px, py, c)
            for t, (chip, _) in enumerate(from_chip):
                part = rows_of(t, c) if t in split else slice(None)
                mv.send(ins[t].at[part], outs[t].at[part], peer, landing=outs[t].at[part],
                        send_if=_is_chip(x, y, chip), recv_if=_is_chip(px, py, chip))
            for t in range(n2):
                mv.send(ins[n1 + t], outs[n1 + t].at[me], peer, landing=outs[n1 + t].at[2 * px + py])

    def relay(mv, ins, outs):
        x, y, c = _position()
        for t in split:
            came = jnp.logical_not(_is_chip(x, y, from_chip[t][0]))
            mv.send(outs[t].at[rows_of(t, c)], outs[t].at[rows_of(t, c)], (x, y, 1 - c),
                    landing=outs[t].at[rows_of(t, 1 - c)], send_if=came, recv_if=came)

    arrays = [a for _, a in from_chip] + list(from_all)
    shapes = [jax.ShapeDtypeStruct(a.shape, a.dtype) for _, a in from_chip]
    shapes += [jax.ShapeDtypeStruct((N_CHIPS,) + a.shape, a.dtype) for a in from_all]
    return _Exchange(arrays, shapes, len(CHIP_FLIPS) * (n1 + n2), n1 + n2, build,
                     n_relay=len(split), relay=relay if split else None)


def _scatter_exchange(to_chip, to_all):
    n1, n2 = len(to_chip), len(to_all)

    def build(mv, ins, outs):
        x, y, c = _position()
        for f, (fx, fy) in enumerate(CHIP_FLIPS):
            px, py = _flip(x, fx), _flip(y, fy)
            peer = (px, py, c)
            for t, (chip, _) in enumerate(to_chip):
                mv.send(ins[t], outs[t].at[f], peer, landing=outs[t].at[f],
                        send_if=_is_chip(px, py, chip), recv_if=_is_chip(x, y, chip))
            for t in range(n2):
                mv.send(ins[n1 + t].at[2 * px + py], outs[n1 + t].at[f], peer, landing=outs[n1 + t].at[f])

    arrays = [a for _, a in to_chip] + list(to_all)
    shapes = [jax.ShapeDtypeStruct((len(CHIP_FLIPS),) + a.shape, a.dtype) for _, a in to_chip]
    shapes += [jax.ShapeDtypeStruct((len(CHIP_FLIPS),) + a.shape[1:], a.dtype) for a in to_all]
    return _Exchange(arrays, shapes, len(CHIP_FLIPS) * (n1 + n2), 0, build)


def _swap_sibling(tensors, name):
    n = len(tensors)

    def body(*refs):
        ins, outs = refs[:n], refs[n:2 * n]
        send_sems, recv_sems = refs[2 * n:]
        x, y, c = _position()
        copies = [pltpu.make_async_remote_copy(
            src_ref=ins[t], dst_ref=outs[t], send_sem=send_sems.at[t], recv_sem=recv_sems.at[t],
            device_id=(x, y, 1 - c), device_id_type=MESH) for t in range(n)]
        for cp in copies:
            cp.start()
        for cp in copies:
            cp.wait_recv()
        for cp in copies:
            cp.wait_send()

    return pl.pallas_call(
        body, name=name, in_specs=[ANY] * n, out_specs=[ANY] * n,
        out_shape=[jax.ShapeDtypeStruct(a.shape, a.dtype) for a in tensors],
        scratch_shapes=[pltpu.SemaphoreType.DMA((n,)), pltpu.SemaphoreType.DMA((n,))],
        compiler_params=pltpu.CompilerParams(has_side_effects=True),
    )(*tensors)


def _allreduce_small(slab):
    stages = 3

    def body(x_ref, o_ref, buf, send_sems, recv_sems):
        x, y, c = _position()
        peers = ((1 - x, y, c), (x, 1 - y, c), (x, y, 1 - c))
        o_ref[...] = x_ref[...]
        for k, peer in enumerate(peers):
            cp = pltpu.make_async_remote_copy(src_ref=o_ref, dst_ref=buf.at[k], send_sem=send_sems.at[k],
                                              recv_sem=recv_sems.at[k], device_id=peer, device_id_type=MESH)
            cp.start()
            cp.wait()
            o_ref[...] = o_ref[...] + buf[k]

    return pl.pallas_call(
        body, name="allreduce_small",
        in_specs=[pl.BlockSpec(memory_space=pltpu.VMEM)], out_specs=pl.BlockSpec(memory_space=pltpu.VMEM),
        out_shape=jax.ShapeDtypeStruct(slab.shape, slab.dtype),
        scratch_shapes=[pltpu.VMEM((stages,) + slab.shape, slab.dtype),
                        pltpu.SemaphoreType.DMA((stages,)), pltpu.SemaphoreType.DMA((stages,))],
        compiler_params=pltpu.CompilerParams(has_side_effects=True),
    )(slab)


def _row_tile(r):
    return min(r, 256)


def _sum4(stack, recv, me):
    _, r, c = stack.shape
    tr = _row_tile(r)

    def body(me_ref, own_ref, recv_ref, o_ref):
        o_ref[...] = (((own_ref[...] + recv_ref[0].astype(F32)) + recv_ref[1].astype(F32))
                      + recv_ref[2].astype(F32))

    return pl.pallas_call(
        body, name="sum_partials",
        grid_spec=pltpu.PrefetchScalarGridSpec(
            num_scalar_prefetch=1, grid=(r // tr,),
            in_specs=[pl.BlockSpec((None, tr, c), lambda i, me_ref: (me_ref[0], i, 0)),
                      pl.BlockSpec((len(CHIP_FLIPS), tr, c), lambda i, me_ref: (0, i, 0))],
            out_specs=pl.BlockSpec((tr, c), lambda i, me_ref: (i, 0))),
        out_shape=jax.ShapeDtypeStruct((r, c), F32), compiler_params=_params("parallel"),
    )(me, stack, recv)


def _sum_block(own, recv):
    r, c = own.shape
    tr = _row_tile(r)

    def body(own_ref, recv_ref, o_ref):
        o_ref[...] = (((own_ref[...] + recv_ref[0].astype(F32)) + recv_ref[1].astype(F32))
                      + recv_ref[2].astype(F32))

    return pl.pallas_call(
        body, name="sum_block", grid=(r // tr,),
        in_specs=[pl.BlockSpec((tr, c), lambda i: (i, 0)), pl.BlockSpec((len(CHIP_FLIPS), tr, c), lambda i: (0, i, 0))],
        out_specs=pl.BlockSpec((tr, c), lambda i: (i, 0)),
        out_shape=jax.ShapeDtypeStruct((r, c), F32), compiler_params=_params("parallel"),
    )(own, recv)


def _adamw_math(w, g, m, v):
    m = ADAM_B1 * m + (1.0 - ADAM_B1) * g
    v = ADAM_B2 * v + (1.0 - ADAM_B2) * (g * g)
    m_hat = m / (1.0 - ADAM_B1 ** ADAM_STEP)
    v_hat = v / (1.0 - ADAM_B2 ** ADAM_STEP)
    delta = -ADAM_LR * (m_hat / (jnp.sqrt(v_hat) + ADAM_EPS) + ADAM_WD * w)
    return delta, m, v


def _adamw(w, m, v, g_parts, name):
    r, c = w.shape
    tr = _row_tile(r)
    n = len(g_parts)

    def body(*refs):
        w_ref, m_ref, v_ref = refs[:3]
        g_refs = refs[3:3 + n]
        g_out, d_out, m_out, v_out = refs[3 + n:]
        g = g_refs[0][...]
        for ref in g_refs[1:]:
            g = g + ref[...]
        g_out[...] = g
        d_out[...], m_out[...], v_out[...] = _adamw_math(w_ref[...], g, m_ref[...], v_ref[...])

    blk = pl.BlockSpec((tr, c), lambda i: (i, 0))
    return pl.pallas_call(
        body, name=name, grid=(r // tr,), in_specs=[blk] * (3 + n), out_specs=[blk] * 4,
        out_shape=[jax.ShapeDtypeStruct((r, c), F32)] * 4, compiler_params=_params("parallel"),
    )(w, m, v, *g_parts)


def _adamw_small(total, w, m, v):
    sizes = [w[n].size for n in SMALL]
    flat = lambda d: [d[n].reshape(1, -1) for n in SMALL]
    k = len(SMALL)

    def body(*refs):
        total_ref, w_refs, m_refs, v_refs = refs[0], refs[1:1 + k], refs[1 + k:1 + 2 * k], refs[1 + 2 * k:1 + 3 * k]
        outs = refs[1 + 3 * k:]
        for i, size in enumerate(sizes):
            g = total_ref[i:i + 1, 0:size]
            outs[i][...] = g
            outs[k + i][...], outs[2 * k + i][...], outs[3 * k + i][...] = _adamw_math(
                w_refs[i][...], g, m_refs[i][...], v_refs[i][...])

    res = pl.pallas_call(
        body, name="adamw_small", out_shape=[jax.ShapeDtypeStruct((1, size), F32) for size in sizes] * 4,
        compiler_params=_params(),
    )(total, *flat(w), *flat(m), *flat(v))
    return [{n: res[j * k + i].reshape(w[n].shape) for i, n in enumerate(SMALL)} for j in range(4)]


SHARDED = ("w_in", "w_lora_up", "a_lora_up", "w_out_a", "w_out_b", "w_out")
ROW_SHARDED = ("w_out",)
SMALL = ("norm_g", "shift_mu", "w0", "a0", "k_k", "k_a", "r_k", "lnx_w", "lnx_b", "f_bias", "q_norm_g", "k_norm_g",
         "final_norm_g")
WEIGHTS = ("norm_g", "w_in", "shift_mu", "w_lora_up", "w0", "a_lora_up", "a0", "k_k", "k_a", "r_k", "lnx_w", "lnx_b",
           "f_bias", "q_norm_g", "k_norm_g", "w_out_a", "w_out_b", "w_out", "final_norm_g")
SLAB_ROWS = 16
SLAB_COLS = SEC


def _to_slab(named, extra=None):
    rows = [jnp.pad(named[n].reshape(1, -1), ((0, 0), (0, SLAB_COLS - named[n].size))) for n in SMALL]
    if extra is not None:
        rows.append(jnp.pad(extra.reshape(1, -1), ((0, 0), (0, SLAB_COLS - extra.size))))
    rows.append(jnp.zeros((SLAB_ROWS - len(rows), SLAB_COLS), F32))
    return jnp.concatenate(rows, axis=0)


def _by_chip(g, name):
    if name in ROW_SHARDED:
        return g.reshape(N_CHIPS, g.shape[0] // N_CHIPS, g.shape[1])
    r, c = g.shape
    return g.reshape(r, N_CHIPS, c // N_CHIPS).transpose(1, 0, 2)


def _from_chips(stack, name):
    if name in ROW_SHARDED:
        return stack.reshape(-1, stack.shape[2])
    _, r, c = stack.shape
    return stack.transpose(1, 0, 2).reshape(r, N_CHIPS * c)


def kernel(x, norm_g, w_in, shift_mu, w_lora_up, w0, a_lora_up, a0, k_k, k_a, r_k, lnx_w, lnx_b, f_bias, q_norm_g, k_norm_g, w_out_a, w_out_b, w_out, final_norm_g, loss_target, m_norm_g, m_w_in, m_shift_mu, m_w_lora_up, m_w0, m_a_lora_up, m_a0, m_k_k, m_k_a, m_r_k, m_lnx_w, m_lnx_b, m_f_bias, m_q_norm_g, m_k_norm_g, m_w_out_a, m_w_out_b, m_w_out, m_final_norm_g, v_norm_g, v_w_in, v_shift_mu, v_w_lora_up, v_w0, v_a_lora_up, v_a0, v_k_k, v_k_a, v_r_k, v_lnx_w, v_lnx_b, v_f_bias, v_q_norm_g, v_k_norm_g, v_w_out_a, v_w_out_b, v_w_out, v_final_norm_g):
    w = dict(norm_g=norm_g, w_in=w_in, shift_mu=shift_mu, w_lora_up=w_lora_up, w0=w0, a_lora_up=a_lora_up, a0=a0,
             k_k=k_k, k_a=k_a, r_k=r_k, lnx_w=lnx_w, lnx_b=lnx_b, f_bias=f_bias, q_norm_g=q_norm_g,
             k_norm_g=k_norm_g, w_out_a=w_out_a, w_out_b=w_out_b, w_out=w_out, final_norm_g=final_norm_g)
    m = dict(norm_g=m_norm_g, w_in=m_w_in, shift_mu=m_shift_mu, w_lora_up=m_w_lora_up, w0=m_w0,
             a_lora_up=m_a_lora_up, a0=m_a0, k_k=m_k_k, k_a=m_k_a, r_k=m_r_k, lnx_w=m_lnx_w, lnx_b=m_lnx_b,
             f_bias=m_f_bias, q_norm_g=m_q_norm_g, k_norm_g=m_k_norm_g, w_out_a=m_w_out_a, w_out_b=m_w_out_b,
             w_out=m_w_out, final_norm_g=m_final_norm_g)
    v = dict(norm_g=v_norm_g, w_in=v_w_in, shift_mu=v_shift_mu, w_lora_up=v_w_lora_up, w0=v_w0,
             a_lora_up=v_a_lora_up, a0=v_a0, k_k=v_k_k, k_a=v_k_a, r_k=v_r_k, lnx_w=v_lnx_w, lnx_b=v_lnx_b,
             f_bias=v_f_bias, q_norm_g=v_q_norm_g, k_norm_g=v_k_norm_g, w_out_a=v_w_out_a, w_out_b=v_w_out_b,
             w_out=v_w_out, final_norm_g=v_final_norm_g)
    shapes = {n: w[n].shape for n in WEIGHTS}

    shard = {n: w[n][0].astype(BF16) for n in SHARDED}
    late = ("w_out_a", "w_out_b", "w_out")
    loras = ("w_lora_up", "a_lora_up")
    w_in_head, w_in_tail = shard["w_in"][:, :A_TAIL], shard["w_in"][:, A_TAIL:]
    shard0, shard1_head, up_stack, aup_stack = _gather_exchange(
        [(0, shard["w_in"]), (1, w_in_head)], [shard[n] for n in loras], split=(0,)).run_alone("gather_early")
    w_a = jnp.concatenate([shard0, shard1_head], axis=1)

    def late_weights(arrived):
        shard1_tail, shard2, shard3 = arrived[:3]
        w_b = jnp.concatenate([shard1_tail, shard2[:, :B_TAIL], jnp.zeros((D_MODEL, SEC - FOX_REAL), BF16)], axis=1)
        w_g = jnp.concatenate([shard2[:, B_TAIL:], shard3], axis=1)
        return (w_b, w_g, *[_from_chips(s, n) for n, s in zip(late, arrived[3:])])

    own = {}

    def bwd_exchange(dw_b, dw_g, dwa, dwb, dwo):
        own["tail1"] = dw_b[:, :B_HEAD]
        own["block2"] = jnp.concatenate([dw_b[:, B_HEAD:FOX_REAL], dw_g[:, :G_HEAD]], axis=1)
        own["block3"] = dw_g[:, G_HEAD:]
        own.update({n: _by_chip(g, n) for n, g in zip(late, (dwa, dwb, dwo))})
        return _scatter_exchange([(1, own["tail1"].astype(BF16)), (2, own["block2"].astype(BF16)),
                                  (3, own["block3"].astype(BF16))], [own[n].astype(BF16) for n in late])

    def tail_exchange(dw_a, dw_up, da_up):
        own["block0"], own["head1"] = dw_a[:, :SHARD_COLS], dw_a[:, SHARD_COLS:]
        own.update({n: _by_chip(g, n) for n, g in zip(loras, (dw_up, da_up))})
        return _scatter_exchange([(0, own["block0"].astype(BF16)), (1, own["head1"].astype(BF16))],
                                 [own[n].astype(BF16) for n in loras])

    small = {n: w[n] for n in SMALL}
    loss_vec, grad_x, grads, sent, sent_last = _device_grads(
        x[0], loss_target[0], small, w_a, _from_chips(up_stack, "w_lora_up"), _from_chips(aup_stack, "a_lora_up"),
        late_weights, _gather_exchange([(1, w_in_tail), (2, shard["w_in"]), (3, shard["w_in"])], [shard[n] for n in late]),
        bwd_exchange, tail_exchange)

    total = _allreduce_small(_to_slab(grads, extra=loss_vec))
    loss = (0.5 / D_MODEL) * jnp.sum(total[len(SMALL)])
    out_g, out_d, out_m, out_v = _adamw_small(total, w, m, v)

    xpos, ypos, _ = _position()
    me = (2 * xpos + ypos).astype(jnp.int32).reshape(1)
    core_sum = {n: _sum4(own[n], r, me) for n, r in zip(late, sent[3:])}
    theirs = dict(zip(late, _swap_sibling([core_sum[n] for n in late], "swap_sibling_early")))
    sent_last, out_d["norm_g"], theirs = lax.optimization_barrier((sent_last, out_d["norm_g"], theirs))
    core_sum["w_in"] = lax.switch(me[0], [
        lambda: _sum_block(own["block0"], sent_last[0]),
        lambda: jnp.concatenate([_sum_block(own["head1"], sent_last[1]), _sum_block(own["tail1"], sent[0])], axis=1),
        lambda: _sum_block(own["block2"], sent[1]),
        lambda: _sum_block(own["block3"], sent[2])])
    core_sum.update({n: _sum4(own[n], r, me) for n, r in zip(loras, sent_last[2:])})
    rest = ("w_in",) + loras
    theirs.update(zip(rest, _swap_sibling([core_sum[n] for n in rest], "swap_sibling")))
    for n in SHARDED:
        g, d, m2, v2 = _adamw(w[n][0], m[n][0], v[n][0], [core_sum[n], theirs[n]], "adamw_" + n)
        out_g[n], out_d[n], out_m[n], out_v[n] = (a.reshape(shapes[n]) for a in (g, d, m2, v2))

    return (loss, grad_x.reshape(x.shape), *[out_g[n] for n in WEIGHTS], *[out_d[n] for n in WEIGHTS],
            *[out_m[n] for n in WEIGHTS], *[out_v[n] for n in WEIGHTS])
```

```python
import functools
import math

import jax
import jax.numpy as jnp
from jax import lax
from jax.experimental import pallas as pl
from jax.experimental.pallas import tpu as pltpu
from jax.experimental.pallas import tpu_sc as plsc

F32 = jnp.float32
BF16 = jnp.bfloat16

D_MODEL = 1024
D_HALF = 512
HEAD = 64
N_HEADS = 8
LORA = 64
RWKV_COLS = 2176
FOX_REAL = 2056
SEC = 2176
GATE_COLS = 2048
IN_COLS = 6280
N_CHIPS = 4
SHARD_COLS = IN_COLS // N_CHIPS
A_TAIL = RWKV_COLS - SHARD_COLS
B_HEAD = SHARD_COLS - A_TAIL
B_TAIL = FOX_REAL - B_HEAD
G_HEAD = SHARD_COLS - B_TAIL
RMS_EPS = 1e-6
LNX_EPS = 64e-5
ATT_SCALE = HEAD ** -0.5
NEG = -1e30

ADAM_LR = 0.001
ADAM_B1 = 0.9
ADAM_B2 = 0.999
ADAM_EPS = 1e-08
ADAM_WD = 0.01
ADAM_STEP = 10

LANES = 128
SUBLANES = 8
VMEM_LIMIT = 56 * 1024 * 1024
MESH = pl.DeviceIdType.MESH


def _params(*sem):
    return pltpu.CompilerParams(dimension_semantics=sem if sem else None, vmem_limit_bytes=VMEM_LIMIT)


def _sigmoid(x):
    return 1.0 / (1.0 + jnp.exp(-x))


def _log_sigmoid(x):
    return jnp.minimum(x, 0.0) - jnp.log(1.0 + jnp.exp(-jnp.abs(x)))


def _head_ones():
    r = lax.broadcasted_iota(jnp.int32, (LANES, LANES), 0) >> 6
    c = lax.broadcasted_iota(jnp.int32, (LANES, LANES), 1) >> 6
    return (r == c).astype(BF16)


def _split3(x):
    hi = x.astype(BF16)
    r1 = x - hi.astype(F32)
    mid = r1.astype(BF16)
    lo = (r1 - mid.astype(F32)).astype(BF16)
    return hi, mid, lo


def _exact_dot(x, ones_bf16, ones_first=False):
    out = None
    for piece in _split3(x):
        if ones_first:
            t = jnp.dot(ones_bf16, piece, preferred_element_type=F32)
        else:
            t = jnp.dot(piece, ones_bf16, preferred_element_type=F32)
        out = t if out is None else out + t
    return out


def _head_sum(x, bd):
    n = x.shape[1] // LANES
    parts = [_exact_dot(x[:, i * LANES:(i + 1) * LANES], bd) for i in range(n)]
    return parts[0] if n == 1 else jnp.concatenate(parts, axis=1)


def _dot_nt(a, b):
    return lax.dot_general(a, b, (((1,), (1,)), ((), ())), preferred_element_type=F32)


def _dot_tn(a, b):
    return lax.dot_general(a, b, (((0,), (0,)), ((), ())), preferred_element_type=F32)


def _colsum(x):
    return jnp.sum(x, axis=0, keepdims=True)


def _rmsnorm_in(x, g, tm=512):
    s, d = x.shape

    def body(x_ref, g_ref, h_ref):
        xv = x_ref[...]
        r = lax.rsqrt(jnp.mean(xv * xv, axis=-1, keepdims=True) + RMS_EPS)
        h_ref[...] = (xv * r * g_ref[...]).astype(BF16)

    return pl.pallas_call(
        body, name="rmsnorm_in", grid=(s // tm,),
        in_specs=[pl.BlockSpec((tm, d), lambda i: (i, 0)), pl.BlockSpec((1, d), lambda i: (0, 0))],
        out_specs=pl.BlockSpec((tm, d), lambda i: (i, 0)),
        out_shape=jax.ShapeDtypeStruct((s, d), BF16), compiler_params=_params("parallel"),
    )(x, g)


def _matmul_nn(a, b, name, tm=512):
    m, k = a.shape
    n = b.shape[1]

    def body(a_ref, b_ref, o_ref):
        o_ref[...] = jnp.dot(a_ref[...], b_ref[...], preferred_element_type=F32)

    return pl.pallas_call(
        body, name=name, grid=(m // tm,),
        in_specs=[pl.BlockSpec((tm, k), lambda i: (i, 0)), pl.BlockSpec((k, n), lambda i: (0, 0))],
        out_specs=pl.BlockSpec((tm, n), lambda i: (i, 0)),
        out_shape=jax.ShapeDtypeStruct((m, n), F32), compiler_params=_params("parallel"),
    )(a, b)


def _matmul_tn_acc(at, b, name, tk=512):
    m, k = at.shape
    n = b.shape[1]

    def body(a_ref, b_ref, o_ref):
        j = pl.program_id(0)

        @pl.when(j == 0)
        def _():
            o_ref[...] = jnp.zeros_like(o_ref)

        o_ref[...] += jnp.dot(a_ref[...], b_ref[...].astype(BF16), preferred_element_type=F32)

    return pl.pallas_call(
        body, name=name, grid=(k // tk,),
        in_specs=[pl.BlockSpec((m, tk), lambda j: (0, j)), pl.BlockSpec((tk, n), lambda j: (j, 0))],
        out_specs=pl.BlockSpec((m, n), lambda j: (0, 0)),
        out_shape=jax.ShapeDtypeStruct((m, n), F32), compiler_params=_params("arbitrary"),
    )(at, b)


def _inproj_bwd(du_a, du_b, du_g, w_a, w_b, w_g, x, dx2, g, exchange=None, tm=256):
    s, d = x.shape
    nb = s // tm

    def body(*refs):
        ((da_ref, db_ref, dg_ref, wa_ref, wb_ref, wg_ref, x_ref, dx2_ref, g_ref), (gx_ref, gg_ref), _,
         moves) = _split_refs(refs, 9, 2, exchange)
        i = pl.program_id(0)
        if moves:
            moves.start(also=(i == 0))

        @pl.when(i == 0)
        def _():
            gg_ref[...] = jnp.zeros_like(gg_ref)

        dh = _dot_nt(da_ref[...].astype(BF16), wa_ref[...])
        dh += _dot_nt(db_ref[...].astype(BF16), wb_ref[...])
        dh += _dot_nt(dg_ref[...].astype(BF16), wg_ref[...])
        xv = x_ref[...]
        r = lax.rsqrt(jnp.mean(xv * xv, axis=-1, keepdims=True) + RMS_EPS)
        xh = xv * r
        gg_ref[...] += _colsum(dh * xh)
        dxh = dh * g_ref[...]
        gx_ref[...] = dx2_ref[...] + r * (dxh - xh * jnp.mean(dxh * xh, axis=-1, keepdims=True))
        if moves:
            moves.wait(also=(i == nb - 1))

    row = lambda w: pl.BlockSpec((tm, w), lambda i: (i, 0))
    full = lambda a: pl.BlockSpec(a.shape, lambda i: (0, 0))
    ex_in = exchange.operands if exchange else []
    ex_out = exchange.out_shapes if exchange else []
    res = pl.pallas_call(
        body, name="inproj_bwd", grid=(nb,),
        in_specs=[row(SEC), row(SEC), row(GATE_COLS), full(w_a), full(w_b), full(w_g), row(d), row(d), full(g)]
                 + [ANY] * len(ex_in),
        out_specs=[row(d), pl.BlockSpec((1, d), lambda i: (0, 0))] + [ANY] * len(ex_out),
        out_shape=[jax.ShapeDtypeStruct((s, d), F32), jax.ShapeDtypeStruct((1, d), F32)] + ex_out,
        scratch_shapes=exchange.scratch() if exchange else [],
        compiler_params=_params("arbitrary"),
    )(du_a, du_b, du_g, w_a, w_b, w_g, x, dx2, g, *ex_in)
    return res[0], res[1], list(res[2:])


def _rwkv_elementwise(ua, prev_row, first, mu, wl, w0, a0, kkw, kaw, bd):
    tm = ua.shape[0]
    rows = lax.broadcasted_iota(jnp.int32, (tm, 1), 0)
    prev = jnp.where(first, jnp.zeros_like(prev_row), prev_row)
    shifted = jnp.where(rows == 0, prev, pltpu.roll(ua, 1, 0))
    delta = shifted - ua
    us = ua + delta * mu
    r = us[:, 0:512]
    k0 = us[:, 512:1024]
    v = us[:, 1024:1536]
    lo = us[:, 1536:1664]
    gate = us[:, 1664:2176]
    lane = lax.broadcasted_iota(jnp.int32, (1, LANES), 1)
    th = jnp.tanh(lo)
    lin = jnp.where(lane < LORA, th, lo)
    ll = jnp.dot(lin.astype(BF16), wl, preferred_element_type=F32)
    sz = _sigmoid(w0 + ll[:, :512])
    e = sz * math.exp(-0.5)
    dec = jnp.exp(-e)
    a = _sigmoid(a0 + ll[:, 512:])
    kk0 = k0 * kkw
    ss = _head_sum(kk0 * kk0, bd)
    nrm = jnp.maximum(jnp.sqrt(ss), 1e-12)
    kk = kk0 / nrm
    k = k0 * (1.0 + (a - 1.0) * kaw)
    return dict(delta=delta, us=us, r=r, k0=k0, v=v, lo=lo, gate=gate, th=th, lin=lin, sz=sz, e=e, dec=dec,
                a=a, kk0=kk0, ss=ss, nrm=nrm, kk=kk, k=k)


def _rwkv_prep(u_a, mu, wl, w0, a0, kkw, kaw, tm=256):
    s = u_a.shape[0]

    def body(ua_ref, prev_ref, mu_ref, wl_ref, w0_ref, a0_ref, kkw_ref, kaw_ref,
             r_ref, w_ref, k_ref, v_ref, a_ref, b_ref, g_ref):
        i = pl.program_id(0)
        f = _rwkv_elementwise(ua_ref[...], prev_ref[7:8, :], i == 0, mu_ref[...], wl_ref[...], w0_ref[...],
                              a0_ref[...], kkw_ref[...], kaw_ref[...], _head_ones())
        r_ref[...] = f["r"]
        w_ref[...] = f["dec"]
        k_ref[...] = f["k"]
        v_ref[...] = f["v"]
        a_ref[...] = -f["kk"]
        b_ref[...] = f["kk"] * f["a"]
        g_ref[...] = f["gate"]

    vec = lambda w: pl.BlockSpec((1, w), lambda i: (0, 0))
    out = pl.BlockSpec((tm, D_HALF), lambda i: (i, 0))
    return pl.pallas_call(
        body, name="rwkv_prep", grid=(s // tm,),
        in_specs=[pl.BlockSpec((tm, SEC), lambda i: (i, 0)),
                  pl.BlockSpec((8, SEC), lambda i: (jnp.maximum(i * (tm // 8) - 1, 0), 0)),
                  vec(SEC), pl.BlockSpec((LANES, 2 * D_HALF), lambda i: (0, 0)),
                  vec(D_HALF), vec(D_HALF), vec(D_HALF), vec(D_HALF)],
        out_specs=[out] * 7,
        out_shape=[jax.ShapeDtypeStruct((s, D_HALF), F32)] * 7,
        compiler_params=_params("parallel"),
    )(u_a, u_a, mu, wl, w0, a0, kkw, kaw)


SCAN_TB = 128
N_PAIRS = 4


def _pair_sum(x, left):
    s_l = jnp.sum(jnp.where(left, x, 0.0), axis=1, keepdims=True)
    s_r = jnp.sum(jnp.where(left, 0.0, x), axis=1, keepdims=True)
    return jnp.where(left, s_l, s_r)


def _pair_dot(x, row_l, row_r, left):
    s_l = jnp.sum(x * row_l, axis=1, keepdims=True)
    s_r = jnp.sum(x * row_r, axis=1, keepdims=True)
    return jnp.where(left, s_l, s_r)


def _halves(rows8):
    lane = lax.broadcasted_iota(jnp.int32, rows8.shape, 1)
    keep_left = (lane & (LANES - 1)) < HEAD
    return jnp.where(keep_left, rows8, 0.0), jnp.where(keep_left, 0.0, rows8)


def _quad_consts():
    lane = lax.broadcasted_iota(jnp.int32, (HEAD, 2 * LANES), 1)
    rowi = lax.broadcasted_iota(jnp.int32, (HEAD, 2 * LANES), 0)
    diag2 = rowi == (lane & (HEAD - 1))
    r = lax.broadcasted_iota(jnp.int32, (2 * LANES, 2 * LANES), 0) >> 6
    c = lax.broadcasted_iota(jnp.int32, (2 * LANES, 2 * LANES), 1) >> 6
    return diag2, (r == c).astype(BF16)


def _rows_to_columns(x8, diag2, bd2):
    lhs = jnp.concatenate([jnp.where(diag2, x8[i:i + 1], 0.0).astype(BF16) for i in range(SUBLANES)], axis=0)
    return jnp.dot(lhs, bd2, preferred_element_type=F32)


def _diag_rows(qtile, diag2, bd2, sub_row2):
    res = jnp.dot(qtile, bd2, preferred_element_type=F32)
    out = jnp.zeros((SUBLANES, 2 * LANES), F32)
    for i in range(SUBLANES):
        out = jnp.where(sub_row2 == i, _colsum(jnp.where(diag2, res[i * HEAD:(i + 1) * HEAD], 0.0)), out)
    return out


def _store_tile(qbuf, slot, p, i, x):
    qbuf[slot, p // 2, i * HEAD:(i + 1) * HEAD, (p % 2) * LANES:(p % 2 + 1) * LANES] = x.astype(BF16)


def _left_half():
    return lax.broadcasted_iota(jnp.int32, (HEAD, LANES), 1) < HEAD


def _split_refs(refs, n_rows, n_out, exchange):
    n_in = len(exchange.operands) if exchange else 0
    n_ex_out = len(exchange.out_shapes) if exchange else 0
    refs = list(refs)
    rows, refs = refs[:n_rows], refs[n_rows:]
    ex_in, refs = refs[:n_in], refs[n_in:]
    outs, refs = refs[:n_out], refs[n_out:]
    ex_out, refs = refs[:n_ex_out], refs[n_ex_out:]
    scratch, sems = (refs[:-3], refs[-3:]) if exchange else (refs, None)
    moves = exchange.moves(ex_in, ex_out, sems) if exchange else None
    return rows, outs, scratch, moves


def _wkv_fwd(r, w, k, a, b, v, exchange=None):
    s = r.shape[0]
    tb = SCAN_TB
    nb = s // tb

    def body(*refs):
        (r_ref, w_ref, k_ref, a_ref, b_ref, v_ref), (y_ref, st_ref), (state, vbuf, qbuf), moves = _split_refs(
            refs, 6, 2, exchange)
        g = pl.program_id(0)
        if moves:
            moves.start(also=(g == 0))

        @pl.when(g == 0)
        def _():
            state[...] = jnp.zeros_like(state)
            qbuf[...] = jnp.zeros_like(qbuf)

        left = _left_half()
        diag2, bd2 = _quad_consts()
        sub_row2 = lax.broadcasted_iota(jnp.int32, (SUBLANES, 2 * LANES), 0)
        groups = tb // SUBLANES
        quads = [slice(g2 * 2 * LANES, (g2 + 1) * 2 * LANES) for g2 in range(2)]

        def rows_of(q):
            return pl.ds(pl.multiple_of(q * SUBLANES, SUBLANES), SUBLANES)

        def v_tiles(q, slot):
            v8 = v_ref[rows_of(q), :]
            for g2 in range(2):
                vbuf[slot, g2] = _rows_to_columns(v8[:, quads[g2]], diag2, bd2)

        def chain(q, slot):
            rows8 = rows_of(q)
            a8, w8, b8, k8, r8 = (x[rows8, :] for x in (a_ref, w_ref, b_ref, k_ref, r_ref))
            pairs = [slice(p * LANES, (p + 1) * LANES) for p in range(N_PAIRS)]
            a_next = pltpu.roll(a8, SUBLANES - 1, 0)
            (a8_l, a8_r), (wa8_l, wa8_r) = _halves(a8), _halves(w8 * a_next)
            ba8 =jnp.concatenate([_pair_sum(b8[:, pr] * a_next[:, pr], left[0:SUBLANES]) for pr in pairs], axis=1)
            ka8 = jnp.concatenate([_pair_sum(k8[:, pr] * a_next[:, pr], left[0:SUBLANES]) for pr in pairs], axis=1)
            sp = [state[p] for p in range(N_PAIRS)]
            for i in range(0, SUBLANES, 2):
                r0, r1 = slice(i, i + 1), slice(i + 1, i + 2)
                sums = [(_pair_dot(sp[p], a8_l[r0, pairs[p]], a8_r[r0, pairs[p]], left),
                         _pair_dot(sp[p], wa8_l[r0, pairs[p]], wa8_r[r0, pairs[p]], left)) for p in range(N_PAIRS)]
                sa0, sa1 = [s[0] for s in sums], [s[1] for s in sums]
                for p in range(N_PAIRS):
                    pr = pairs[p]
                    inner = slice((p % 2) * LANES, (p % 2 + 1) * LANES)
                    vt0 = vbuf[slot, p // 2, i * HEAD:(i + 1) * HEAD, inner]
                    vt1 = vbuf[slot, p // 2, (i + 1) * HEAD:(i + 2) * HEAD, inner]
                    sa_next = sa1[p] + sa0[p] * ba8[r0, pr] + vt0 * ka8[r0, pr]
                    s1 = sp[p] * w8[r0, pr] + sa0[p] * b8[r0, pr] + vt0 * k8[r0, pr]
                    st_ref[q * SUBLANES + i, p] = s1
                    _store_tile(qbuf, slot, p, i, s1 * r8[r0, pr])
                    s2 = s1 * w8[r1, pr] + sa_next * b8[r1, pr] + vt1 * k8[r1, pr]
                    st_ref[q * SUBLANES + i + 1, p] = s2
                    _store_tile(qbuf, slot, p, i + 1, s2 * r8[r1, pr])
                    sp[p] = s2
            for p in range(N_PAIRS):
                state[p] = sp[p]

        def y_rows(q, slot):
            for g2 in range(2):
                y_ref[rows_of(q), quads[g2]] = _diag_rows(qbuf[slot, g2], diag2, bd2, sub_row2)

        v_tiles(0, 0)

        def two_groups(j, carry):
            q0 = 2 * j
            v_tiles(q0 + 1, 1)
            chain(q0, 0)
            y_rows(jnp.maximum(q0 - 1, 0), 1)
            v_tiles(jnp.minimum(q0 + 2, groups - 1), 0)
            chain(q0 + 1, 1)
            y_rows(q0, 0)
            return carry

        lax.fori_loop(0, groups // 2, two_groups, 0)
        y_rows(groups - 1, 1)
        if moves:
            moves.wait(also=(g == nb - 1))

    rows = pl.BlockSpec((tb, D_HALF), lambda g: (g, 0))
    ex_in = exchange.operands if exchange else []
    ex_out = exchange.out_shapes if exchange else []
    res = pl.pallas_call(
        body, name="wkv_fwd", grid=(nb,),
        in_specs=[rows] * 6 + [ANY] * len(ex_in),
        out_specs=[rows, pl.BlockSpec((tb, N_PAIRS, HEAD, LANES), lambda g: (g, 0, 0, 0))] + [ANY] * len(ex_out),
        out_shape=[jax.ShapeDtypeStruct((s, D_HALF), F32),
                   jax.ShapeDtypeStruct((s, N_PAIRS, HEAD, LANES), F32)] + ex_out,
        scratch_shapes=[pltpu.VMEM((N_PAIRS, HEAD, LANES), F32),
                        pltpu.VMEM((2, 2, SUBLANES * HEAD, 2 * LANES), F32),
                        pltpu.VMEM((2, 2, SUBLANES * HEAD, 2 * LANES), BF16)]
                       + (exchange.scratch() if exchange else []),
        compiler_params=_params("arbitrary"),
    )(r, w, k, a, b, v, *ex_in)
    return res[0], res[1], list(res[2:])


def _wkv_bwd(r, w, k, a, b, v, dy, st, exchange=None):
    s = r.shape[0]
    tb = SCAN_TB
    nb = s // tb

    def body(*refs):
        ((r_ref, w_ref, k_ref, a_ref, b_ref, v_ref, dy_ref, st_ref, before_ref),
         (dr_ref, dw_ref, dk_ref, dv_ref, da_ref, db_ref), (dstate, vbuf, qbuf, sbuf),
         moves) = _split_refs(refs, 9, 6, exchange)
        g = pl.program_id(0)
        first_block = g == nb - 1
        if moves:
            moves.start(also=(g == 0))

        @pl.when(g == 0)
        def _():
            dstate[...] = jnp.zeros_like(dstate)
            qbuf[...] = jnp.zeros_like(qbuf)

        left = _left_half()
        diag2, bd2 = _quad_consts()
        sub_row = lax.broadcasted_iota(jnp.int32, (SUBLANES, LANES), 0)
        sub_row2 = lax.broadcasted_iota(jnp.int32, (SUBLANES, 2 * LANES), 0)
        groups = tb // SUBLANES
        quads = [slice(g2 * 2 * LANES, (g2 + 1) * 2 * LANES) for g2 in range(2)]
        row_refs = (dr_ref, dw_ref, dk_ref, da_ref, db_ref)

        def rows_of(q):
            return pl.ds(pl.multiple_of(q * SUBLANES, SUBLANES), SUBLANES)

        def state_before(q, i, p):
            if i > 0:
                return st_ref[q * SUBLANES + i - 1, p]
            return jnp.where(q == 0, jnp.where(first_block, 0.0, before_ref[0, p]),
                             st_ref[jnp.maximum(q * SUBLANES - 1, 0), p])

        def column_tiles(q, slot):
            rows8 = rows_of(q)
            for kind, ref in enumerate((v_ref, dy_ref)):
                x8 = ref[rows8, :]
                for g2 in range(2):
                    vbuf[slot, kind, g2] = _rows_to_columns(x8[:, quads[g2]], diag2, bd2)
            a8 = a_ref[rows8, :]
            for i in range(SUBLANES):
                for p in range(N_PAIRS):
                    _store_tile(sbuf, 0, p, i, state_before(q, i, p) * a8[i:i + 1, p * LANES:(p + 1) * LANES])
            for g2 in range(2):
                vbuf[slot, 2, g2] = jnp.dot(sbuf[0, g2], bd2, preferred_element_type=F32)

        def chain(q, slot):
            rows8 = rows_of(q)
            a8, w8, b8, k8, r8 = (x[rows8, :] for x in (a_ref, w_ref, b_ref, k_ref, r_ref))
            b8_l, b8_r = _halves(b8)
            dsp = [dstate[p] for p in range(N_PAIRS)]
            outs = [[jnp.zeros((SUBLANES, LANES), F32) for _ in row_refs] for _ in range(N_PAIRS)]
            after = [st_ref[q * SUBLANES + SUBLANES - 1, p] for p in range(N_PAIRS)]
            for i in reversed(range(SUBLANES)):
                row = slice(i, i + 1)
                pl_ = [slice(p * LANES, (p + 1) * LANES) for p in range(N_PAIRS)]
                tile = [(p // 2, slice(i * HEAD, (i + 1) * HEAD), slice((p % 2) * LANES, (p % 2 + 1) * LANES))
                        for p in range(N_PAIRS)]
                sp = [state_before(q, i, p) for p in range(N_PAIRS)]
                dyt = [vbuf[(slot, 1) + tile[p]] for p in range(N_PAIRS)]
                ds = [dsp[p] + dyt[p] * r8[row, pl_[p]] for p in range(N_PAIRS)]
                dsa = [_pair_dot(ds[p], b8_l[row, pl_[p]], b8_r[row, pl_[p]], left) for p in range(N_PAIRS)]
                sa = [vbuf[(slot, 2) + tile[p]] for p in range(N_PAIRS)]
                for p in range(N_PAIRS):
                    ar, wr, br, kr = (x[row, pl_[p]] for x in (a8, w8, b8, k8))
                    vt = vbuf[(slot, 0) + tile[p]]
                    dsp[p] = ds[p] * wr + dsa[p] * ar
                    new = (_colsum(after[p] * dyt[p]), _colsum(ds[p] * sp[p]), _colsum(ds[p] * vt),
                           _colsum(sp[p] * dsa[p]), _colsum(ds[p] * sa[p]))
                    outs[p] = [jnp.where(sub_row == i, n, o) for n, o in zip(new, outs[p])]
                    _store_tile(qbuf, slot, p, i, ds[p] * kr)
                after = sp
            for p in range(N_PAIRS):
                dstate[p] = dsp[p]
                for ref, o in zip(row_refs, outs[p]):
                    ref[rows8, p * LANES:(p + 1) * LANES] = o

        def dv_rows(q, slot):
            for g2 in range(2):
                dv_ref[rows_of(q), quads[g2]] = _diag_rows(qbuf[slot, g2], diag2, bd2, sub_row2)

        column_tiles(groups - 1, 0)

        def two_groups(j, carry):
            q0 = groups - 1 - 2 * j
            column_tiles(q0 - 1, 1)
            chain(q0, 0)
            dv_rows(jnp.minimum(q0 + 1, groups - 1), 1)
            column_tiles(jnp.maximum(q0 - 2, 0), 0)
            chain(q0 - 1, 1)
            dv_rows(q0, 0)
            return carry

        lax.fori_loop(0, groups // 2, two_groups, 0)
        dv_rows(0, 1)
        if moves:
            moves.wait(also=(g == nb - 1))

    rows = pl.BlockSpec((tb, D_HALF), lambda g: (nb - 1 - g, 0))
    ex_in = exchange.operands if exchange else []
    ex_out = exchange.out_shapes if exchange else []
    res = pl.pallas_call(
        body, name="wkv_bwd", grid=(nb,),
        in_specs=[rows] * 7 + [pl.BlockSpec((tb, N_PAIRS, HEAD, LANES), lambda g: (nb - 1 - g, 0, 0, 0)),
                               pl.BlockSpec((1, N_PAIRS, HEAD, LANES),
                                            lambda g: (jnp.maximum((nb - 1 - g) * tb - 1, 0), 0, 0, 0))]
                 + [ANY] * len(ex_in),
        out_specs=[rows] * 6 + [ANY] * len(ex_out),
        out_shape=[jax.ShapeDtypeStruct((s, D_HALF), F32)] * 6 + ex_out,
        scratch_shapes=[pltpu.VMEM((N_PAIRS, HEAD, LANES), F32),
                        pltpu.VMEM((2, 3, 2, SUBLANES * HEAD, 2 * LANES), F32),
                        pltpu.VMEM((2, 2, SUBLANES * HEAD, 2 * LANES), BF16),
                        pltpu.VMEM((1, 2, SUBLANES * HEAD, 2 * LANES), BF16)]
                       + (exchange.scratch() if exchange else []),
        compiler_params=_params("arbitrary"),
    )(r, w, k, a, b, v, dy, st, st, *ex_in)
    return list(res[:6]), list(res[6:])


def _rwkv_post_math(y, r, k, v, gate, lw, lb, rk, bd):
    mean = _head_sum(y, bd) * (1.0 / HEAD)
    yc = y - mean
    var = _head_sum(yc * yc, bd) * (1.0 / HEAD)
    rstd = lax.rsqrt(var + LNX_EPS)
    yn = yc * rstd
    rkk = _head_sum(r * k * rk, bd)
    sg = _sigmoid(gate)
    pre = yn * lw + lb + rkk * v
    return yn, rstd, rkk, sg, pre


def _rwkv_post(y, r, k, v, gate, lw, lb, rk, tm=256):
    s = y.shape[0]

    def body(y_ref, r_ref, k_ref, v_ref, g_ref, lw_ref, lb_ref, rk_ref, o_ref):
        gate_v = g_ref[...]
        _, _, _, sg, pre = _rwkv_post_math(y_ref[...], r_ref[...], k_ref[...], v_ref[...], gate_v,
                                           lw_ref[...], lb_ref[...], rk_ref[...], _head_ones())
        o_ref[...] = pre * (gate_v * sg)

    blk = pl.BlockSpec((tm, D_HALF), lambda i: (i, 0))
    vec = pl.BlockSpec((1, D_HALF), lambda i: (0, 0))
    return pl.pallas_call(
        body, name="rwkv_post", grid=(s // tm,),
        in_specs=[blk] * 5 + [vec] * 3, out_specs=blk,
        out_shape=jax.ShapeDtypeStruct((s, D_HALF), F32), compiler_params=_params("parallel"),
    )(y, r, k, v, gate, lw, lb, rk)


def _rwkv_post_bwd(dmix, y, r, k, v, gate, lw, lb, rk, tm=256):
    s = y.shape[0]

    def body(dm_ref, y_ref, r_ref, k_ref, v_ref, g_ref, lw_ref, lb_ref, rk_ref,
             dy_ref, dr_ref, dk_ref, dv_ref, dg_ref, dlw_ref, dlb_ref, drk_ref):
        i = pl.program_id(0)

        @pl.when(i == 0)
        def _():
            dlw_ref[...] = jnp.zeros_like(dlw_ref)
            dlb_ref[...] = jnp.zeros_like(dlb_ref)
            drk_ref[...] = jnp.zeros_like(drk_ref)

        bd = _head_ones()
        rv, kv, vv, gate_v, lw_v, rk_v = r_ref[...], k_ref[...], v_ref[...], g_ref[...], lw_ref[...], rk_ref[...]
        yn, rstd, rkk, sg, pre = _rwkv_post_math(y_ref[...], rv, kv, vv, gate_v, lw_v, lb_ref[...], rk_v, bd)
        dm = dm_ref[...]
        dg_ref[...] = dm * pre * (sg * (1.0 + gate_v * (1.0 - sg)))
        dpre = dm * (gate_v * sg)
        dlw_ref[...] += _colsum(dpre * yn)
        dlb_ref[...] += _colsum(dpre)
        dyn = dpre * lw_v
        m1 = _head_sum(dyn, bd) * (1.0 / HEAD)
        m2 = _head_sum(dyn * yn, bd) * (1.0 / HEAD)
        dy_ref[...] = rstd * (dyn - m1 - yn * m2)
        dv_ref[...] = dpre * rkk
        drkk = _head_sum(dpre * vv, bd)
        dr_ref[...] = drkk * kv * rk_v
        dk_ref[...] = drkk * rv * rk_v
        drk_ref[...] += _colsum(drkk * rv * kv)

    blk = pl.BlockSpec((tm, D_HALF), lambda i: (i, 0))
    vec = pl.BlockSpec((1, D_HALF), lambda i: (0, 0))
    return pl.pallas_call(
        body, name="rwkv_post_bwd", grid=(s // tm,),
        in_specs=[blk] * 6 + [vec] * 3, out_specs=[blk] * 5 + [vec] * 3,
        out_shape=[jax.ShapeDtypeStruct((s, D_HALF), F32)] * 5 + [jax.ShapeDtypeStruct((1, D_HALF), F32)] * 3,
        compiler_params=_params("arbitrary"),
    )(dmix, y, r, k, v, gate, lw, lb, rk)


def _rwkv_prep_bwd(u_a, grads, mu, wl, w0, a0, kkw, kaw, tm=256):
    s = u_a.shape[0]
    nb = s // tm

    def body(ua_ref, prev_ref, drs_ref, dws_ref, dks_ref, dvs_ref, das_ref, dbs_ref, drb_ref, dkb_ref, dvb_ref,
             dgt_ref, mu_ref, wl_ref, w0_ref, a0_ref, kkw_ref, kaw_ref,
             du_ref, dmu_ref, dwl_ref, dw0_ref, da0_ref, dkkw_ref, dkaw_ref, carry):
        i = pl.program_id(0)

        @pl.when(i == 0)
        def _():
            carry[...] = jnp.zeros_like(carry)
            for ref in (dmu_ref, dwl_ref, dw0_ref, da0_ref, dkkw_ref, dkaw_ref):
                ref[...] = jnp.zeros_like(ref)

        bd = _head_ones()
        mu_v, wl_v, kkw_v, kaw_v = mu_ref[...], wl_ref[...], kkw_ref[...], kaw_ref[...]
        f = _rwkv_elementwise(ua_ref[...], prev_ref[7:8, :], i == nb - 1, mu_v, wl_v, w0_ref[...],
                              a0_ref[...], kkw_v, kaw_v, bd)
        a, kk, k0 = f["a"], f["kk"], f["k0"]
        dk = dks_ref[...] + dkb_ref[...]
        dbs = dbs_ref[...]
        dkk = dbs * a - das_ref[...]
        da = dbs * kk + dk * k0 * kaw_v
        dk0 = dk * (1.0 + (a - 1.0) * kaw_v)
        dkaw_ref[...] += _colsum(dk * k0 * (a - 1.0))
        inv = 1.0 / f["nrm"]
        proj = _head_sum(dkk * kk, bd)
        dkk0 = jnp.where(f["ss"] > 1e-24, (dkk - kk * proj) * inv, dkk * inv)
        dk0 = dk0 + dkk0 * kkw_v
        dkkw_ref[...] += _colsum(dkk0 * k0)
        dza = da * a * (1.0 - a)
        da0_ref[...] += _colsum(dza)
        dz = -dws_ref[...] * f["dec"] * f["e"] * (1.0 - f["sz"])
        dw0_ref[...] += _colsum(dz)
        dll = jnp.concatenate([dz, dza], axis=1).astype(BF16)
        dwl_ref[...] += _dot_tn(f["lin"].astype(BF16), dll)
        dlin = _dot_nt(dll, wl_v)
        lane = lax.broadcasted_iota(jnp.int32, (1, LANES), 1)
        th = f["th"]
        dlo = jnp.where(lane < LORA, dlin * (1.0 - th * th), dlin)
        dus = jnp.concatenate([drs_ref[...] + drb_ref[...], dk0, dvs_ref[...] + dvb_ref[...], dlo, dgt_ref[...]],
                              axis=1)
        dmu_ref[...] += _colsum(dus * f["delta"])
        g1 = dus * mu_v
        rows = lax.broadcasted_iota(jnp.int32, (tm, 1), 0)
        up = jnp.where(rows == tm - 1, carry[...], pltpu.roll(g1, tm - 1, 0))
        du_ref[...] = dus - g1 + up
        carry[...] = g1[0:1, :]

    rev = lambda w: pl.BlockSpec((tm, w), lambda i: (nb - 1 - i, 0))
    vec = lambda w: pl.BlockSpec((1, w), lambda i: (0, 0))
    wl_spec = pl.BlockSpec((LANES, 2 * D_HALF), lambda i: (0, 0))
    return pl.pallas_call(
        body, name="rwkv_prep_bwd", grid=(nb,),
        in_specs=[rev(SEC), pl.BlockSpec((8, SEC), lambda i: (jnp.maximum((nb - 1 - i) * (tm // 8) - 1, 0), 0))]
                 + [rev(D_HALF)] * 10 + [vec(SEC), wl_spec] + [vec(D_HALF)] * 4,
        out_specs=[rev(SEC), vec(SEC), wl_spec] + [vec(D_HALF)] * 4,
        out_shape=[jax.ShapeDtypeStruct((s, SEC), F32), jax.ShapeDtypeStruct((1, SEC), F32),
                   jax.ShapeDtypeStruct((LANES, 2 * D_HALF), F32)] + [jax.ShapeDtypeStruct((1, D_HALF), F32)] * 4,
        scratch_shapes=[pltpu.VMEM((1, SEC), F32)],
        compiler_params=_params("arbitrary"),
    )(u_a, u_a, *grads, mu, wl, w0, a0, kkw, kaw)


def _tri(tm, lower):
    r = lax.broadcasted_iota(jnp.int32, (tm, tm), 0)
    c = lax.broadcasted_iota(jnp.int32, (tm, tm), 1)
    return ((r >= c) if lower else (r <= c)).astype(BF16)


def _head_rms(x, g, bd):
    rinv = lax.rsqrt(_head_sum(x * x, bd) * (1.0 / HEAD) + RMS_EPS)
    xh = x * rinv
    return xh, rinv, xh * g


def _fox_prep(u_b, fb, qg, kg, tm=256):
    s = u_b.shape[0]

    def body(ub_ref, fb_ref, qg_ref, kg_ref, q_ref, k_ref, v_ref, cc_ref, cr_ref, carry):
        i = pl.program_id(0)

        @pl.when(i == 0)
        def _():
            carry[...] = jnp.zeros_like(carry)

        bd = _head_ones()
        _, _, qn = _head_rms(ub_ref[:, 0:512], qg_ref[...], bd)
        _, _, kn = _head_rms(ub_ref[:, 512:1024], kg_ref[...], bd)
        q_ref[...] = (qn * ATT_SCALE).astype(BF16)
        k_ref[...] = kn.astype(BF16)
        v_ref[...] = ub_ref[:, 1024:1536].astype(BF16)
        lane = lax.broadcasted_iota(jnp.int32, (1, LANES), 1)
        logf = jnp.where(lane < N_HEADS, _log_sigmoid(ub_ref[:, 2048:2176] + fb_ref[...]), 0.0)
        cum = _exact_dot(logf, _tri(tm, True), ones_first=True) + carry[...]
        for h in range(N_HEADS):
            cc_ref[h] = jnp.broadcast_to(cum[:, h:h + 1], (tm, LANES))
        cr_ref[...] = jnp.transpose(cum)[0:N_HEADS, :]
        carry[...] = cum[tm - 1:tm, :]

    blk = pl.BlockSpec((tm, D_HALF), lambda i: (i, 0))
    return pl.pallas_call(
        body, name="fox_prep", grid=(s // tm,),
        in_specs=[pl.BlockSpec((tm, SEC), lambda i: (i, 0)), pl.BlockSpec((1, LANES), lambda i: (0, 0)),
                  pl.BlockSpec((1, D_HALF), lambda i: (0, 0)), pl.BlockSpec((1, D_HALF), lambda i: (0, 0))],
        out_specs=[blk, blk, blk, pl.BlockSpec((N_HEADS, tm, LANES), lambda i: (0, i, 0)),
                   pl.BlockSpec((N_HEADS, tm), lambda i: (0, i))],
        out_shape=[jax.ShapeDtypeStruct((s, D_HALF), BF16)] * 3
                  + [jax.ShapeDtypeStruct((N_HEADS, s, LANES), F32), jax.ShapeDtypeStruct((N_HEADS, s), F32)],
        scratch_shapes=[pltpu.VMEM((1, LANES), F32)],
        compiler_params=_params("arbitrary"),
    )(u_b, fb, qg, kg)


ATT_T = 256


def _attn_fwd(q, k, v, cc, cr, u_b):
    s = q.shape[0]
    t = ATT_T
    nblk = s // t

    def body(q_ref, k_ref, v_ref, cc_ref, cr_ref, g_ref, o_ref, mix_ref, lse_ref, m_sc, l_sc, acc_sc):
        i = pl.program_id(0)
        j = pl.program_id(1)

        @pl.when(j == 0)
        def _():
            m_sc[...] = jnp.full_like(m_sc, NEG)
            l_sc[...] = jnp.zeros_like(l_sc)
            acc_sc[...] = jnp.zeros_like(acc_sc)

        def tile(on_diagonal):
            causal = _causal_tile(t) if on_diagonal else None
            left = lax.broadcasted_iota(jnp.int32, (1, LANES), 1) < HEAD
            for p in range(N_PAIRS):
                lanes = slice(p * LANES, (p + 1) * LANES)
                q2, k2, v2 = q_ref[:, lanes], k_ref[:, lanes], v_ref[:, lanes]
                acc2 = acc_sc[:, lanes]
                for e in range(2):
                    h = 2 * p + e
                    msk = left if e == 0 else jnp.logical_not(left)
                    sc = _dot_nt(jnp.where(msk, q2, jnp.zeros_like(q2)), k2)
                    sc = sc + (_wide(cc_ref[h]) - cr_ref[h:h + 1, :])
                    if on_diagonal:
                        sc = jnp.where(causal, sc, NEG)
                    m_prev = m_sc[h]
                    m_new = jnp.maximum(m_prev, jnp.max(sc, axis=1, keepdims=True))
                    alpha = jnp.exp(m_prev - m_new)
                    pm = jnp.exp(sc - _wide(m_new))
                    l_sc[h] = alpha * l_sc[h] + jnp.sum(pm, axis=1, keepdims=True)
                    m_sc[h] = m_new
                    pv = jnp.dot(pm.astype(BF16), v2, preferred_element_type=F32)
                    acc2 = jnp.where(msk, alpha * acc2 + pv, acc2)
                acc_sc[:, lanes] = acc2

        pl.when(j < i)(functools.partial(tile, False))
        pl.when(j == i)(functools.partial(tile, True))

        @pl.when(j == i)
        def _():
            left = lax.broadcasted_iota(jnp.int32, (1, LANES), 1) < HEAD
            for p in range(N_PAIRS):
                lanes = slice(p * LANES, (p + 1) * LANES)
                inv = jnp.where(left, 1.0 / l_sc[2 * p], 1.0 / l_sc[2 * p + 1])
                o = acc_sc[:, lanes] * inv
                o_ref[:, lanes] = o
                gate = g_ref[:, lanes]
                mix_ref[:, lanes] = o * (gate * _sigmoid(gate))
            for h in range(N_HEADS):
                lse_ref[h] = m_sc[h] + jnp.log(l_sc[h])

    qblk = pl.BlockSpec((t, D_HALF), lambda i, j: (i, 0))
    kblk = pl.BlockSpec((t, D_HALF), lambda i, j: (jnp.minimum(i, j), 0))
    return pl.pallas_call(
        body, name="fox_attn_fwd", grid=(nblk, nblk),
        in_specs=[qblk, kblk, kblk, pl.BlockSpec((N_HEADS, t, LANES), lambda i, j: (0, i, 0)),
                  pl.BlockSpec((N_HEADS, t), lambda i, j: (0, jnp.minimum(i, j))),
                  pl.BlockSpec((t, D_HALF), lambda i, j: (i, 3))],
        out_specs=[qblk, qblk, pl.BlockSpec((N_HEADS, t, LANES), lambda i, j: (0, i, 0))],
        out_shape=[jax.ShapeDtypeStruct((s, D_HALF), F32), jax.ShapeDtypeStruct((s, D_HALF), F32),
                   jax.ShapeDtypeStruct((N_HEADS, s, LANES), F32)],
        scratch_shapes=[pltpu.VMEM((N_HEADS, t, LANES), F32), pltpu.VMEM((N_HEADS, t, LANES), F32),
                        pltpu.VMEM((t, D_HALF), F32)],
        compiler_params=_params("parallel", "arbitrary"),
    )(q, k, v, cc, cr, u_b)


def _fox_post_bwd(dmix, o, u_b, tm=256):
    s = o.shape[0]

    def body(dm_ref, o_ref, g_ref, do_ref, dg_ref):
        gate = g_ref[...]
        sg = _sigmoid(gate)
        dm = dm_ref[...]
        do_ref[...] = (dm * (gate * sg)).astype(BF16)
        dg_ref[...] = dm * o_ref[...] * (sg * (1.0 + gate * (1.0 - sg)))

    blk = pl.BlockSpec((tm, D_HALF), lambda i: (i, 0))
    return pl.pallas_call(
        body, name="fox_post_bwd", grid=(s // tm,),
        in_specs=[blk, blk, pl.BlockSpec((tm, D_HALF), lambda i: (i, 3))], out_specs=[blk] * 2,
        out_shape=[jax.ShapeDtypeStruct((s, D_HALF), BF16), jax.ShapeDtypeStruct((s, D_HALF), F32)],
        compiler_params=_params("parallel"),
    )(dmix, o, u_b)


def _causal_tile(t):
    return lax.broadcasted_iota(jnp.int32, (t, t), 0) >= lax.broadcasted_iota(jnp.int32, (t, t), 1)


def _wide(x):
    return jnp.concatenate([x, x], axis=1)


def _attn_probs(q2, k2, v2, do2, msk, causal, bias, lse_rows):
    zero = jnp.zeros_like(q2)
    qh = jnp.where(msk, q2, zero)
    doh = jnp.where(msk, do2, zero)
    sc = _dot_nt(qh, k2) + bias
    if causal is not None:
        sc = jnp.where(causal, sc, NEG)
    pm = jnp.exp(sc - _wide(lse_rows))
    dp = _dot_nt(doh, v2)
    return qh, doh, pm, dp


def _attn_bwd_rowdot(q, k, v, do, lse, cc, cr):
    s = q.shape[0]
    t = ATT_T
    nblk = s // t

    def body(q_ref, k_ref, v_ref, do_ref, lse_ref, cc_ref, cr_ref, dd_ref, acc):
        i = pl.program_id(0)
        j = pl.program_id(1)

        @pl.when(j == 0)
        def _():
            acc[...] = jnp.zeros_like(acc)

        def tile(on_diagonal):
            causal = _causal_tile(t) if on_diagonal else None
            left = lax.broadcasted_iota(jnp.int32, (1, LANES), 1) < HEAD
            for p in range(N_PAIRS):
                lanes = slice(p * LANES, (p + 1) * LANES)
                q2, k2, v2, do2 = q_ref[:, lanes], k_ref[:, lanes], v_ref[:, lanes], do_ref[:, lanes]
                for e in range(2):
                    h = 2 * p + e
                    msk = left if e == 0 else jnp.logical_not(left)
                    bias = _wide(cc_ref[h]) - cr_ref[h:h + 1, :]
                    _, _, pm, dp = _attn_probs(q2, k2, v2, do2, msk, causal, bias, lse_ref[h])
                    acc[h] += jnp.sum(pm * dp, axis=1, keepdims=True)

        pl.when(j < i)(functools.partial(tile, False))
        pl.when(j == i)(functools.partial(tile, True))

        @pl.when(j == i)
        def _():
            dd_ref[...] = acc[...]

    qblk = pl.BlockSpec((t, D_HALF), lambda i, j: (i, 0))
    qcol = pl.BlockSpec((N_HEADS, t, LANES), lambda i, j: (0, i, 0))
    kblk = pl.BlockSpec((t, D_HALF), lambda i, j: (jnp.minimum(i, j), 0))
    return pl.pallas_call(
        body, name="fox_attn_rowdot", grid=(nblk, nblk),
        in_specs=[qblk, kblk, kblk, qblk, qcol, qcol, pl.BlockSpec((N_HEADS, t), lambda i, j: (0, jnp.minimum(i, j)))],
        out_specs=qcol, out_shape=jax.ShapeDtypeStruct((N_HEADS, s, LANES), F32),
        scratch_shapes=[pltpu.VMEM((N_HEADS, t, LANES), F32)],
        compiler_params=_params("parallel", "arbitrary"),
    )(q, k, v, do, lse, cc, cr)


def _attn_bwd(q, k, v, do, lse, dd, cc, cr):
    s = q.shape[0]
    t = ATT_T
    nblk = s // t

    def body(q_ref, k_ref, v_ref, do_ref, lse_ref, dd_ref, cc_ref, cr_ref,
             dq_ref, dk_ref, dv_ref, dcr_ref, dk_sc, dv_sc, dcr_sc):
        j = pl.program_id(0)
        i = pl.program_id(1)

        @pl.when(jnp.logical_and(j == 0, i == 0))
        def _():
            dq_ref[...] = jnp.zeros_like(dq_ref)

        @pl.when(i == 0)
        def _():
            dk_sc[...] = jnp.zeros_like(dk_sc)
            dv_sc[...] = jnp.zeros_like(dv_sc)
            dcr_sc[...] = jnp.zeros_like(dcr_sc)

        def tile(on_diagonal):
            causal = _causal_tile(t) if on_diagonal else None
            left = lax.broadcasted_iota(jnp.int32, (1, LANES), 1) < HEAD
            qrows = pl.ds(pl.multiple_of(i * t, t), t)
            for p in range(N_PAIRS):
                lanes = slice(p * LANES, (p + 1) * LANES)
                q2, k2, v2, do2 = q_ref[:, lanes], k_ref[:, lanes], v_ref[:, lanes], do_ref[:, lanes]
                zero = jnp.zeros_like(q2)
                dq2 = jnp.zeros((t, LANES), F32)
                dk2 = jnp.zeros((t, LANES), F32)
                dv2 = jnp.zeros((t, LANES), F32)
                for e in range(2):
                    h = 2 * p + e
                    msk = left if e == 0 else jnp.logical_not(left)
                    bias = _wide(cc_ref[h]) - cr_ref[h:h + 1, :]
                    qh, doh, pm, dp = _attn_probs(q2, k2, v2, do2, msk, causal, bias, lse_ref[h])
                    dsc = pm * (dp - _wide(dd_ref[h]))
                    dsb = dsc.astype(BF16)
                    dv2 += _dot_tn(pm.astype(BF16), doh)
                    dk2 += _dot_tn(dsb, qh)
                    dq2 += jnp.dot(dsb, jnp.where(msk, k2, zero), preferred_element_type=F32)
                    dcr_sc[h:h + 1, :] += -_colsum(dsc)
                dq_ref[qrows, lanes] += dq2 * ATT_SCALE
                dk_sc[:, lanes] += dk2
                dv_sc[:, lanes] += dv2

        pl.when(i > j)(functools.partial(tile, False))
        pl.when(i == j)(functools.partial(tile, True))

        @pl.when(i == nblk - 1)
        def _():
            dk_ref[...] = dk_sc[...]
            dv_ref[...] = dv_sc[...]
            dcr_ref[...] = dcr_sc[...]

    qblk = pl.BlockSpec((t, D_HALF), lambda j, i: (jnp.maximum(i, j), 0))
    qcol = pl.BlockSpec((N_HEADS, t, LANES), lambda j, i: (0, jnp.maximum(i, j), 0))
    kblk = pl.BlockSpec((t, D_HALF), lambda j, i: (j, 0))
    return pl.pallas_call(
        body, name="fox_attn_bwd", grid=(nblk, nblk),
        in_specs=[qblk, kblk, kblk, qblk, qcol, qcol, qcol, pl.BlockSpec((N_HEADS, t), lambda j, i: (0, j))],
        out_specs=[pl.BlockSpec((s, D_HALF), lambda j, i: (0, 0)), kblk, kblk,
                   pl.BlockSpec((N_HEADS, t), lambda j, i: (0, j))],
        out_shape=[jax.ShapeDtypeStruct((s, D_HALF), F32)] * 3 + [jax.ShapeDtypeStruct((N_HEADS, s), F32)],
        scratch_shapes=[pltpu.VMEM((t, D_HALF), F32), pltpu.VMEM((t, D_HALF), F32), pltpu.VMEM((N_HEADS, t), F32)],
        compiler_params=_params("arbitrary", "arbitrary"),
    )(q, k, v, do, lse, dd, cc, cr)


def _fox_prep_bwd(u_b, dq, dk, dv, dgate, dcum, fb, qg, kg, tm=256):
    s = u_b.shape[0]
    nb = s // tm

    def body(ub_ref, dq_ref, dk_ref, dv_ref, dg_ref, dc_ref, fb_ref, qg_ref, kg_ref,
             du_ref, dqg_ref, dkg_ref, dfb_ref, carry):
        i = pl.program_id(0)

        @pl.when(i == 0)
        def _():
            carry[...] = jnp.zeros_like(carry)
            dqg_ref[...] = jnp.zeros_like(dqg_ref)
            dkg_ref[...] = jnp.zeros_like(dkg_ref)
            dfb_ref[...] = jnp.zeros_like(dfb_ref)

        bd = _head_ones()
        for lo, g_ref, d_ref, dgain_ref in ((0, qg_ref, dq_ref, dqg_ref), (512, kg_ref, dk_ref, dkg_ref)):
            gain = g_ref[...]
            xh, rinv, _ = _head_rms(ub_ref[:, lo:lo + 512], gain, bd)
            dn = d_ref[...]
            dgain_ref[...] += _colsum(dn * xh)
            dxh = dn * gain
            du_ref[:, lo:lo + 512] = rinv * (dxh - xh * (_head_sum(dxh * xh, bd) * (1.0 / HEAD)))
        du_ref[:, 1024:1536] = dv_ref[...]
        du_ref[:, 1536:2048] = dg_ref[...]
        lane = lax.broadcasted_iota(jnp.int32, (1, LANES), 1)
        dc = dc_ref[...]
        dlogf = _exact_dot(dc, _tri(tm, False), ones_first=True) + carry[...]
        carry[...] += _colsum(dc)
        fl = ub_ref[:, 2048:2176] + fb_ref[...]
        dfl = jnp.where(lane < N_HEADS, dlogf * (1.0 - _sigmoid(fl)), 0.0)
        du_ref[:, 2048:2176] = dfl
        dfb_ref[...] += _colsum(dfl)

    rev = lambda w: pl.BlockSpec((tm, w), lambda i: (nb - 1 - i, 0))
    vec = lambda w: pl.BlockSpec((1, w), lambda i: (0, 0))
    return pl.pallas_call(
        body, name="fox_prep_bwd", grid=(nb,),
        in_specs=[rev(SEC)] + [rev(D_HALF)] * 4 + [rev(LANES), vec(LANES), vec(D_HALF), vec(D_HALF)],
        out_specs=[rev(SEC), vec(D_HALF), vec(D_HALF), vec(LANES)],
        out_shape=[jax.ShapeDtypeStruct((s, SEC), F32), jax.ShapeDtypeStruct((1, D_HALF), F32),
                   jax.ShapeDtypeStruct((1, D_HALF), F32), jax.ShapeDtypeStruct((1, LANES), F32)],
        scratch_shapes=[pltpu.VMEM((1, LANES), F32)],
        compiler_params=_params("arbitrary"),
    )(u_b, dq, dk, dv, dgate, dcum, fb, qg, kg)


def _merge(mix_a, mix_b, u_g, x, tgt, wa, wb, wo, fg, tm=256):
    s, d = x.shape

    def body(ma_ref, mb_ref, ug_ref, x_ref, t_ref, wa_ref, wb_ref, wo_ref, fg_ref,
             dx2_ref, dma_ref, dmb_ref, dug_ref, dwa_ref, dwb_ref, dwo_ref, dfg_ref, loss_ref):
        i = pl.program_id(0)

        @pl.when(i == 0)
        def _():
            for ref in (dwa_ref, dwb_ref, dwo_ref, dfg_ref, loss_ref):
                ref[...] = jnp.zeros_like(ref)

        wa_v, wb_v, wo_v, fg_v = wa_ref[...], wb_ref[...], wo_ref[...], fg_ref[...]
        ma = ma_ref[...].astype(BF16)
        mb = mb_ref[...].astype(BF16)
        ya = jnp.dot(ma, wa_v, preferred_element_type=F32)
        yb = jnp.dot(mb, wb_v, preferred_element_type=F32)
        sa = _sigmoid(ug_ref[:, 0:d])
        sb = _sigmoid(ug_ref[:, d:2 * d])
        merged = (sa * ya + sb * yb).astype(BF16)
        x2 = x_ref[...] + jnp.dot(merged, wo_v, preferred_element_type=F32)
        r2 = lax.rsqrt(jnp.mean(x2 * x2, axis=-1, keepdims=True) + RMS_EPS)
        x2h = x2 * r2
        err = x2h * fg_v - t_ref[...]
        loss_ref[...] += _colsum(err * err)
        dy = err * (1.0 / d)
        dfg_ref[...] += _colsum(dy * x2h)
        dx2h = dy * fg_v
        dx2 = r2 * (dx2h - x2h * jnp.mean(dx2h * x2h, axis=-1, keepdims=True))
        dx2_ref[...] = dx2
        dx2b = dx2.astype(BF16)
        dmerged = _dot_nt(dx2b, wo_v)
        dwo_ref[...] += _dot_tn(merged, dx2b)
        dya = dmerged * sa
        dyb = dmerged * sb
        dug_ref[:, 0:d] = dya * ya * (1.0 - sa)
        dug_ref[:, d:2 * d] = dyb * yb * (1.0 - sb)
        dyab = dya.astype(BF16)
        dybb = dyb.astype(BF16)
        dma_ref[...] = _dot_nt(dyab, wa_v)
        dmb_ref[...] = _dot_nt(dybb, wb_v)
        dwa_ref[...] += _dot_tn(ma, dyab)
        dwb_ref[...] += _dot_tn(mb, dybb)

    row = lambda w: pl.BlockSpec((tm, w), lambda i: (i, 0))
    full = lambda a: pl.BlockSpec(a.shape, lambda i: (0, 0))
    fshape = lambda a: jax.ShapeDtypeStruct(a.shape, F32)
    return pl.pallas_call(
        body, name="merge_fwd_bwd", grid=(s // tm,),
        in_specs=[row(D_HALF), row(D_HALF), row(GATE_COLS), row(d), row(d), full(wa), full(wb), full(wo), full(fg)],
        out_specs=[row(d), row(D_HALF), row(D_HALF), row(GATE_COLS), full(wa), full(wb), full(wo), full(fg), full(fg)],
        out_shape=[jax.ShapeDtypeStruct((s, d), F32), jax.ShapeDtypeStruct((s, D_HALF), F32),
                   jax.ShapeDtypeStruct((s, D_HALF), F32), jax.ShapeDtypeStruct((s, GATE_COLS), F32),
                   fshape(wa), fshape(wb), fshape(wo), fshape(fg), fshape(fg)],
        compiler_params=_params("arbitrary"),
    )(mix_a, mix_b, u_g, x, tgt, wa, wb, wo, fg)


def _lora_weight(w_up, a_up):
    z = jnp.zeros((LORA, D_HALF), w_up.dtype)
    return jnp.concatenate([jnp.concatenate([w_up, z], axis=1), jnp.concatenate([z, a_up], axis=1)], axis=0)


def _device_grads(x, tgt, p, w_a, w_up, a_up, late_weights, fwd_exchange=None, bwd_exchange=None, tail_exchange=None):
    wl = _lora_weight(w_up, a_up)
    rk = p["r_k"].reshape(1, D_HALF)
    fb = jnp.pad(p["f_bias"], ((0, 0), (0, LANES - N_HEADS)))
    qg = jnp.tile(p["q_norm_g"], (1, N_HEADS))
    kg = jnp.tile(p["k_norm_g"], (1, N_HEADS))
    fg = p["final_norm_g"].reshape(1, D_MODEL)
    mixer = (p["shift_mu"], wl, p["w0"], p["a0"], p["k_k"], p["k_a"])

    h = _rmsnorm_in(x, p["norm_g"])
    u_a = _matmul_nn(h, w_a, "inproj_rwkv")
    r, dec, k, v, av, bv, gate_a = _rwkv_prep(u_a, *mixer)
    y, st, arrived = _wkv_fwd(r, dec, k, av, bv, v, fwd_exchange)
    mix_a = _rwkv_post(y, r, k, v, gate_a, p["lnx_w"], p["lnx_b"], rk)

    w_b, w_g, w_out_a, w_out_b, w_out = late_weights(arrived)
    u_b = _matmul_nn(h, w_b, "inproj_fox")
    u_g = _matmul_nn(h, w_g, "inproj_gate")
    q, kn, vb, cc, cr = _fox_prep(u_b, fb, qg, kg)
    o, mix_b, lse = _attn_fwd(q, kn, vb, cc, cr, u_b)

    dx2, dmix_a, dmix_b, du_g, dwa, dwb, dwo, dfg, loss_vec = _merge(
        mix_a, mix_b, u_g, x, tgt, w_out_a, w_out_b, w_out, fg)

    do, dgate_b = _fox_post_bwd(dmix_b, o, u_b)
    dd = _attn_bwd_rowdot(q, kn, vb, do, lse, cc, cr)
    dq, dk_att, dv_att, dcr = _attn_bwd(q, kn, vb, do, lse, dd, cc, cr)
    dcum = jnp.pad(dcr.T, ((0, 0), (0, LANES - N_HEADS)))
    du_b, dqg, dkg, dfb = _fox_prep_bwd(u_b, dq, dk_att, dv_att, dgate_b, dcum, fb, qg, kg)
    h_t = h.T
    dw_b = _matmul_tn_acc(h_t, du_b, "dw_fox")
    dw_g = _matmul_tn_acc(h_t, du_g, "dw_gate")

    dy, dr_b, dk_b, dv_b, dgate_a, dlw, dlb, drk = _rwkv_post_bwd(
        dmix_a, y, r, k, v, gate_a, p["lnx_w"], p["lnx_b"], rk)
    scan_grads, sent = _wkv_bwd(r, dec, k, av, bv, v, dy, st,
                                bwd_exchange(dw_b, dw_g, dwa, dwb, dwo) if bwd_exchange else None)
    du_a, dmu, dwl, dw0, da0, dkkw, dkaw = _rwkv_prep_bwd(u_a, (*scan_grads, dr_b, dk_b, dv_b, dgate_a), *mixer)
    dw_a = _matmul_tn_acc(h_t, du_a, "dw_rwkv")
    dw_up, da_up = dwl[:LORA, :D_HALF], dwl[LORA:, D_HALF:]
    sent_last = _run_on_sequencer(tail_exchange(dw_a, dw_up, da_up), "scatter_tail", 1) if tail_exchange else []
    grad_x, dnorm_g, _ = _inproj_bwd(du_a, du_b, du_g, w_a, w_b, w_g, x, dx2, p["norm_g"])

    grads = dict(
        norm_g=dnorm_g, w_in=(dw_a, dw_b, dw_g), shift_mu=dmu,
        w_lora_up=dw_up, w0=dw0, a_lora_up=da_up, a0=da0, k_k=dkkw, k_a=dkaw,
        r_k=drk.reshape(1, N_HEADS, HEAD), lnx_w=dlw, lnx_b=dlb, f_bias=dfb[:, :N_HEADS],
        q_norm_g=dqg.reshape(N_HEADS, HEAD).sum(axis=0, keepdims=True),
        k_norm_g=dkg.reshape(N_HEADS, HEAD).sum(axis=0, keepdims=True),
        w_out_a=dwa, w_out_b=dwb, w_out=dwo, final_norm_g=dfg.reshape(D_MODEL))
    return loss_vec, grad_x, grads, sent, sent_last


CHIP_FLIPS = ((1, 0), (0, 1), (1, 1))
ANY = pl.BlockSpec(memory_space=pl.ANY)


def _position():
    return lax.axis_index("x"), lax.axis_index("y"), lax.axis_index("c")


def _flip(v, f):
    return 1 - v if f else v


def _both(a, b):
    if a is None:
        return b
    return a if b is None else jnp.logical_and(a, b)


def _when(cond, fn):
    if cond is None:
        fn()
    else:
        pl.when(cond)(fn)


class _Moves:
    def __init__(self, send_sems, recv_sems, local_sems):
        self.send_sems, self.recv_sems, self.local_sems = send_sems, recv_sems, local_sems
        self.remote, self.local = [], []

    def send(self, src, dst, peer, landing, send_if=None, recv_if=None):
        k = len(self.remote)
        sems = dict(send_sem=self.send_sems.at[k], recv_sem=self.recv_sems.at[k], device_id=peer, device_id_type=MESH)
        out = pltpu.make_async_remote_copy(src_ref=src, dst_ref=dst, **sems)
        arrival = pltpu.make_async_remote_copy(src_ref=src, dst_ref=landing, **sems)
        self.remote.append((out, arrival, send_if, recv_if))

    def copy(self, src, dst, cond=None):
        cp = pltpu.make_async_copy(src, dst, self.local_sems.at[len(self.local)])
        self.local.append((cp, cond))

    def start(self, also=None):
        for cp, cond in self.local:
            _when(_both(also, cond), cp.start)
        for out, _, send_if, _ in self.remote:
            _when(_both(also, send_if), out.start)

    def wait_arrivals(self, also=None):
        for _, arrival, _, recv_if in self.remote:
            _when(_both(also, recv_if), arrival.wait_recv)

    def wait_sent(self, also=None):
        for out, _, send_if, _ in self.remote:
            _when(_both(also, send_if), out.wait_send)
        for cp, cond in self.local:
            _when(_both(also, cond), cp.wait)

    def wait(self, also=None):
        self.wait_arrivals(also)
        self.wait_sent(also)


class _Exchange:
    def __init__(self, operands, out_shapes, n_remote, n_local, build, n_relay=0, relay=None):
        self.operands, self.out_shapes = list(operands), list(out_shapes)
        self.n_remote, self.n_local, self.build = n_remote, n_local, build
        self.n_relay, self.relay = n_relay, relay

    def scratch(self):
        return [pltpu.SemaphoreType.DMA((self.n_remote,)), pltpu.SemaphoreType.DMA((self.n_remote,)),
                pltpu.SemaphoreType.DMA((max(self.n_local, 1),))]

    def moves(self, in_refs, out_refs, sems):
        mv = _Moves(*sems)
        self.build(mv, in_refs, out_refs)
        return mv

    def run_alone(self, name):
        n_in, n_out = len(self.operands), len(self.out_shapes)
        relay_scratch = [pltpu.SemaphoreType.DMA((self.n_relay,))] * 2 if self.relay else []

        def body(*refs):
            ins, outs, sems = refs[:n_in], refs[n_in:n_in + n_out], refs[n_in + n_out:]
            mv = self.moves(ins, outs, sems[:3])
            mv.start()
            mv.wait_arrivals()
            if self.relay:
                passed = _Moves(sems[3], sems[4], None)
                self.relay(passed, ins, outs)
                passed.start()
                passed.wait()
            mv.wait_sent()

        return pl.pallas_call(
            body, name=name, in_specs=[ANY] * n_in, out_specs=[ANY] * n_out, out_shape=self.out_shapes,
            scratch_shapes=self.scratch() + relay_scratch, compiler_params=pltpu.CompilerParams(has_side_effects=True),
        )(*self.operands)


def _run_on_sequencer(exchange, name, collective_id):
    ins = [jax.new_ref(a, memory_space=pltpu.MemorySpace.HBM) for a in exchange.operands]
    outs = [jax.empty_ref(s, memory_space=pltpu.MemorySpace.HBM) for s in exchange.out_shapes]
    relay_scratch = [pltpu.SemaphoreType.DMA((exchange.n_relay,))] * 2 if exchange.relay else []

    def launch(*sems):
        x, y, c = _position()
        peers = [(_flip(x, fx), _flip(y, fy), c) for fx, fy in CHIP_FLIPS] + ([(x, y, 1 - c)] if exchange.relay else [])
        barrier = pltpu.get_barrier_semaphore()
        for peer in peers:
            pl.semaphore_signal(barrier, inc=1, device_id=peer, device_id_type=MESH)
        pl.semaphore_wait(barrier, len(peers))
        moves = exchange.moves(ins, outs, sems[:3])
        moves.start()
        moves.wait_arrivals()
        if exchange.relay:
            passed = _Moves(sems[3], sems[4], None)
            exchange.relay(passed, ins, outs)
            passed.start()
            passed.wait()
        moves.wait_sent()

    pl.kernel(launch, mesh=plsc.ScalarSubcoreMesh(axis_name="sequencer", num_cores=1), name=name,
              scratch_types=tuple(exchange.scratch() + relay_scratch),
              compiler_params=pltpu.CompilerParams(collective_id=collective_id))()
    return [o[...] for o in outs]


def _is_chip(x, y, chip):
    return jnp.logical_and(x == chip // 2, y == chip % 2)


def _gather_exchange(from_chip, from_all, split=()):
    n1, n2 = len(from_chip), len(from_all)

    def rows_of(t, c):
        half = from_chip[t][1].shape[0] // 2
        return pl.ds(c * half, half)

    def build(mv, ins, outs):
        x, y, c = _position()
        me = 2 * x + y
        for t, (chip, _) in enumerate(from_chip):
            mv.copy(ins[t], outs[t], cond=_is_chip(x, y, chip))
        for t in range(n2):
            mv.copy(ins[n1 + t], outs[n1 + t].at[me])
        for fx, fy in CHIP_FLIPS:
            px, py = _flip(x, fx), _flip(y, fy)
            peer = (px, py, c)
            for t, (chip, _) in enumerate(from_chip):
                part = rows_of(t, c) if t in split else slice(None)
                mv.send(ins[t].at[part], outs[t].at[part], peer, landing=outs[t].at[part],
                        send_if=_is_chip(x, y, chip), recv_if=_is_chip(px, py, chip))
            for t in range(n2):
                mv.send(ins[n1 + t], outs[n1 + t].at[me], peer, landing=outs[n1 + t].at[2 * px + py])

    def relay(mv, ins, outs):
        x, y, c = _position()
        for t in split:
            came = jnp.logical_not(_is_chip(x, y, from_chip[t][0]))
            mv.send(outs[t].at[rows_of(t, c)], outs[t].at[rows_of(t, c)], (x, y, 1 - c),
                    landing=outs[t].at[rows_of(t, 1 - c)], send_if=came, recv_if=came)

    arrays = [a for _, a in from_chip] + list(from_all)
    shapes = [jax.ShapeDtypeStruct(a.shape, a.dtype) for _, a in from_chip]
    shapes += [jax.ShapeDtypeStruct((N_CHIPS,) + a.shape, a.dtype) for a in from_all]
    return _Exchange(arrays, shapes, len(CHIP_FLIPS) * (n1 + n2), n1 + n2, build,
                     n_relay=len(split), relay=relay if split else None)


def _scatter_exchange(to_chip, to_all):
    n1, n2 = len(to_chip), len(to_all)

    def build(mv, ins, outs):
        x, y, c = _position()
        for f, (fx, fy) in enumerate(CHIP_FLIPS):
            px, py = _flip(x, fx), _flip(y, fy)
            peer = (px, py, c)
            for t, (chip, _) in enumerate(to_chip):
                mv.send(ins[t], outs[t].at[f], peer, landing=outs[t].at[f],
                        send_if=_is_chip(px, py, chip), recv_if=_is_chip(x, y, chip))
            for t in range(n2):
                mv.send(ins[n1 + t].at[2 * px + py], outs[n1 + t].at[f], peer, landing=outs[n1 + t].at[f])

    arrays = [a for _, a in to_chip] + list(to_all)
    shapes = [jax.ShapeDtypeStruct((len(CHIP_FLIPS),) + a.shape, a.dtype) for _, a in to_chip]
    shapes += [jax.ShapeDtypeStruct((len(CHIP_FLIPS),) + a.shape[1:], a.dtype) for a in to_all]
    return _Exchange(arrays, shapes, len(CHIP_FLIPS) * (n1 + n2), 0, build)


def _swap_sibling(tensors, name):
    n = len(tensors)

    def body(*refs):
        ins, outs = refs[:n], refs[n:2 * n]
        send_sems, recv_sems = refs[2 * n:]
        x, y, c = _position()
        copies = [pltpu.make_async_remote_copy(
            src_ref=ins[t], dst_ref=outs[t], send_sem=send_sems.at[t], recv_sem=recv_sems.at[t],
            device_id=(x, y, 1 - c), device_id_type=MESH) for t in range(n)]
        for cp in copies:
            cp.start()
        for cp in copies:
            cp.wait_recv()
        for cp in copies:
            cp.wait_send()

    return pl.pallas_call(
        body, name=name, in_specs=[ANY] * n, out_specs=[ANY] * n,
        out_shape=[jax.ShapeDtypeStruct(a.shape, a.dtype) for a in tensors],
        scratch_shapes=[pltpu.SemaphoreType.DMA((n,)), pltpu.SemaphoreType.DMA((n,))],
        compiler_params=pltpu.CompilerParams(has_side_effects=True),
    )(*tensors)


def _allreduce_small(slab):
    stages = 3

    def body(x_ref, o_ref, buf, send_sems, recv_sems):
        x, y, c = _position()
        peers = ((1 - x, y, c), (x, 1 - y, c), (x, y, 1 - c))
        o_ref[...] = x_ref[...]
        for k, peer in enumerate(peers):
            cp = pltpu.make_async_remote_copy(src_ref=o_ref, dst_ref=buf.at[k], send_sem=send_sems.at[k],
                                              recv_sem=recv_sems.at[k], device_id=peer, device_id_type=MESH)
            cp.start()
            cp.wait()
            o_ref[...] = o_ref[...] + buf[k]

    return pl.pallas_call(
        body, name="allreduce_small",
        in_specs=[pl.BlockSpec(memory_space=pltpu.VMEM)], out_specs=pl.BlockSpec(memory_space=pltpu.VMEM),
        out_shape=jax.ShapeDtypeStruct(slab.shape, slab.dtype),
        scratch_shapes=[pltpu.VMEM((stages,) + slab.shape, slab.dtype),
                        pltpu.SemaphoreType.DMA((stages,)), pltpu.SemaphoreType.DMA((stages,))],
        compiler_params=pltpu.CompilerParams(has_side_effects=True),
    )(slab)


def _row_tile(r):
    return min(r, 256)


def _sum4(stack, recv, me):
    _, r, c = stack.shape
    tr = _row_tile(r)

    def body(me_ref, own_ref, recv_ref, o_ref):
        o_ref[...] = (((own_ref[...] + recv_ref[0].astype(F32)) + recv_ref[1].astype(F32))
                      + recv_ref[2].astype(F32))

    return pl.pallas_call(
        body, name="sum_partials",
        grid_spec=pltpu.PrefetchScalarGridSpec(
            num_scalar_prefetch=1, grid=(r // tr,),
            in_specs=[pl.BlockSpec((None, tr, c), lambda i, me_ref: (me_ref[0], i, 0)),
                      pl.BlockSpec((len(CHIP_FLIPS), tr, c), lambda i, me_ref: (0, i, 0))],
            out_specs=pl.BlockSpec((tr, c), lambda i, me_ref: (i, 0))),
        out_shape=jax.ShapeDtypeStruct((r, c), F32), compiler_params=_params("parallel"),
    )(me, stack, recv)


def _sum_block(own, recv):
    r, c = own.shape
    tr = _row_tile(r)

    def body(own_ref, recv_ref, o_ref):
        o_ref[...] = (((own_ref[...] + recv_ref[0].astype(F32)) + recv_ref[1].astype(F32))
                      + recv_ref[2].astype(F32))

    return pl.pallas_call(
        body, name="sum_block", grid=(r // tr,),
        in_specs=[pl.BlockSpec((tr, c), lambda i: (i, 0)), pl.BlockSpec((len(CHIP_FLIPS), tr, c), lambda i: (0, i, 0))],
        out_specs=pl.BlockSpec((tr, c), lambda i: (i, 0)),
        out_shape=jax.ShapeDtypeStruct((r, c), F32), compiler_params=_params("parallel"),
    )(own, recv)


def _adamw_math(w, g, m, v):
    m = ADAM_B1 * m + (1.0 - ADAM_B1) * g
    v = ADAM_B2 * v + (1.0 - ADAM_B2) * (g * g)
    m_hat = m / (1.0 - ADAM_B1 ** ADAM_STEP)
    v_hat = v / (1.0 - ADAM_B2 ** ADAM_STEP)
    delta = -ADAM_LR * (m_hat / (jnp.sqrt(v_hat) + ADAM_EPS) + ADAM_WD * w)
    return delta, m, v


def _adamw(w, m, v, g_parts, name):
    r, c = w.shape
    tr = _row_tile(r)
    n = len(g_parts)

    def body(*refs):
        w_ref, m_ref, v_ref = refs[:3]
        g_refs = refs[3:3 + n]
        g_out, d_out, m_out, v_out = refs[3 + n:]
        g = g_refs[0][...]
        for ref in g_refs[1:]:
            g = g + ref[...]
        g_out[...] = g
        d_out[...], m_out[...], v_out[...] = _adamw_math(w_ref[...], g, m_ref[...], v_ref[...])

    blk = pl.BlockSpec((tr, c), lambda i: (i, 0))
    return pl.pallas_call(
        body, name=name, grid=(r // tr,), in_specs=[blk] * (3 + n), out_specs=[blk] * 4,
        out_shape=[jax.ShapeDtypeStruct((r, c), F32)] * 4, compiler_params=_params("parallel"),
    )(w, m, v, *g_parts)


def _adamw_small(total, w, m, v):
    sizes = [w[n].size for n in SMALL]
    flat = lambda d: [d[n].reshape(1, -1) for n in SMALL]
    k = len(SMALL)

    def body(*refs):
        total_ref, w_refs, m_refs, v_refs = refs[0], refs[1:1 + k], refs[1 + k:1 + 2 * k], refs[1 + 2 * k:1 + 3 * k]
        outs = refs[1 + 3 * k:]
        for i, size in enumerate(sizes):
            g = total_ref[i:i + 1, 0:size]
            outs[i][...] = g
            outs[k + i][...], outs[2 * k + i][...], outs[3 * k + i][...] = _adamw_math(
                w_refs[i][...], g, m_refs[i][...], v_refs[i][...])

    res = pl.pallas_call(
        body, name="adamw_small", out_shape=[jax.ShapeDtypeStruct((1, size), F32) for size in sizes] * 4,
        compiler_params=_params(),
    )(total, *flat(w), *flat(m), *flat(v))
    return [{n: res[j * k + i].reshape(w[n].shape) for i, n in enumerate(SMALL)} for j in range(4)]


SHARDED = ("w_in", "w_lora_up", "a_lora_up", "w_out_a", "w_out_b", "w_out")
ROW_SHARDED = ("w_out",)
SMALL = ("norm_g", "shift_mu", "w0", "a0", "k_k", "k_a", "r_k", "lnx_w", "lnx_b", "f_bias", "q_norm_g", "k_norm_g",
         "final_norm_g")
WEIGHTS = ("norm_g", "w_in", "shift_mu", "w_lora_up", "w0", "a_lora_up", "a0", "k_k", "k_a", "r_k", "lnx_w", "lnx_b",
           "f_bias", "q_norm_g", "k_norm_g", "w_out_a", "w_out_b", "w_out", "final_norm_g")
SLAB_ROWS = 16
SLAB_COLS = SEC


def _to_slab(named, extra=None):
    rows = [jnp.pad(named[n].reshape(1, -1), ((0, 0), (0, SLAB_COLS - named[n].size))) for n in SMALL]
    if extra is not None:
        rows.append(jnp.pad(extra.reshape(1, -1), ((0, 0), (0, SLAB_COLS - extra.size))))
    rows.append(jnp.zeros((SLAB_ROWS - len(rows), SLAB_COLS), F32))
    return jnp.concatenate(rows, axis=0)


def _by_chip(g, name):
    if name in ROW_SHARDED:
        return g.reshape(N_CHIPS, g.shape[0] // N_CHIPS, g.shape[1])
    r, c = g.shape
    return g.reshape(r, N_CHIPS, c // N_CHIPS).transpose(1, 0, 2)


def _from_chips(stack, name):
    if name in ROW_SHARDED:
        return stack.reshape(-1, stack.shape[2])
    _, r, c = stack.shape
    return stack.transpose(1, 0, 2).reshape(r, N_CHIPS * c)


def kernel(x, norm_g, w_in, shift_mu, w_lora_up, w0, a_lora_up, a0, k_k, k_a, r_k, lnx_w, lnx_b, f_bias, q_norm_g, k_norm_g, w_out_a, w_out_b, w_out, final_norm_g, loss_target, m_norm_g, m_w_in, m_shift_mu, m_w_lora_up, m_w0, m_a_lora_up, m_a0, m_k_k, m_k_a, m_r_k, m_lnx_w, m_lnx_b, m_f_bias, m_q_norm_g, m_k_norm_g, m_w_out_a, m_w_out_b, m_w_out, m_final_norm_g, v_norm_g, v_w_in, v_shift_mu, v_w_lora_up, v_w0, v_a_lora_up, v_a0, v_k_k, v_k_a, v_r_k, v_lnx_w, v_lnx_b, v_f_bias, v_q_norm_g, v_k_norm_g, v_w_out_a, v_w_out_b, v_w_out, v_final_norm_g):
    w = dict(norm_g=norm_g, w_in=w_in, shift_mu=shift_mu, w_lora_up=w_lora_up, w0=w0, a_lora_up=a_lora_up, a0=a0,
             k_k=k_k, k_a=k_a, r_k=r_k, lnx_w=lnx_w, lnx_b=lnx_b, f_bias=f_bias, q_norm_g=q_norm_g,
             k_norm_g=k_norm_g, w_out_a=w_out_a, w_out_b=w_out_b, w_out=w_out, final_norm_g=final_norm_g)
    m = dict(norm_g=m_norm_g, w_in=m_w_in, shift_mu=m_shift_mu, w_lora_up=m_w_lora_up, w0=m_w0,
             a_lora_up=m_a_lora_up, a0=m_a0, k_k=m_k_k, k_a=m_k_a, r_k=m_r_k, lnx_w=m_lnx_w, lnx_b=m_lnx_b,
             f_bias=m_f_bias, q_norm_g=m_q_norm_g, k_norm_g=m_k_norm_g, w_out_a=m_w_out_a, w_out_b=m_w_out_b,
             w_out=m_w_out, final_norm_g=m_final_norm_g)
    v = dict(norm_g=v_norm_g, w_in=v_w_in, shift_mu=v_shift_mu, w_lora_up=v_w_lora_up, w0=v_w0,
             a_lora_up=v_a_lora_up, a0=v_a0, k_k=v_k_k, k_a=v_k_a, r_k=v_r_k, lnx_w=v_lnx_w, lnx_b=v_lnx_b,
             f_bias=v_f_bias, q_norm_g=v_q_norm_g, k_norm_g=v_k_norm_g, w_out_a=v_w_out_a, w_out_b=v_w_out_b,
             w_out=v_w_out, final_norm_g=v_final_norm_g)
    shapes = {n: w[n].shape for n in WEIGHTS}

    shard = {n: w[n][0].astype(BF16) for n in SHARDED}
    late = ("w_out_a", "w_out_b", "w_out")
    loras = ("w_lora_up", "a_lora_up")
    w_in_head, w_in_tail = shard["w_in"][:, :A_TAIL], shard["w_in"][:, A_TAIL:]
    shard0, shard1_head, up_stack, aup_stack = _run_on_sequencer(_gather_exchange(
        [(0, shard["w_in"]), (1, w_in_head)], [shard[n] for n in loras], split=(0,)), "gather_early", 2)
    w_a = jnp.concatenate([shard0, shard1_head], axis=1)

    def late_weights(arrived):
        shard1_tail, shard2, shard3 = arrived[:3]
        w_b = jnp.concatenate([shard1_tail, shard2[:, :B_TAIL], jnp.zeros((D_MODEL, SEC - FOX_REAL), BF16)], axis=1)
        w_g = jnp.concatenate([shard2[:, B_TAIL:], shard3], axis=1)
        return (w_b, w_g, *[_from_chips(s, n) for n, s in zip(late, arrived[3:])])

    own = {}

    def bwd_exchange(dw_b, dw_g, dwa, dwb, dwo):
        own["tail1"] = dw_b[:, :B_HEAD]
        own["block2"] = jnp.concatenate([dw_b[:, B_HEAD:FOX_REAL], dw_g[:, :G_HEAD]], axis=1)
        own["block3"] = dw_g[:, G_HEAD:]
        own.update({n: _by_chip(g, n) for n, g in zip(late, (dwa, dwb, dwo))})
        return _scatter_exchange([(1, own["tail1"].astype(BF16)), (2, own["block2"].astype(BF16)),
                                  (3, own["block3"].astype(BF16))], [own[n].astype(BF16) for n in late])

    def tail_exchange(dw_a, dw_up, da_up):
        own["block0"], own["head1"] = dw_a[:, :SHARD_COLS], dw_a[:, SHARD_COLS:]
        own.update({n: _by_chip(g, n) for n, g in zip(loras, (dw_up, da_up))})
        return _scatter_exchange([(0, own["block0"].astype(BF16)), (1, own["head1"].astype(BF16))],
                                 [own[n].astype(BF16) for n in loras])

    small = {n: w[n] for n in SMALL}
    loss_vec, grad_x, grads, sent, sent_last = _device_grads(
        x[0], loss_target[0], small, w_a, _from_chips(up_stack, "w_lora_up"), _from_chips(aup_stack, "a_lora_up"),
        late_weights, _gather_exchange([(1, w_in_tail), (2, shard["w_in"]), (3, shard["w_in"])], [shard[n] for n in late]),
        bwd_exchange, tail_exchange)

    total = _allreduce_small(_to_slab(grads, extra=loss_vec))
    loss = (0.5 / D_MODEL) * jnp.sum(total[len(SMALL)])
    out_g, out_d, out_m, out_v = _adamw_small(total, w, m, v)

    xpos, ypos, _ = _position()
    me = (2 * xpos + ypos).astype(jnp.int32).reshape(1)
    core_sum = {n: _sum4(own[n], r, me) for n, r in zip(late, sent[3:])}
    theirs = dict(zip(late, _swap_sibling([core_sum[n] for n in late], "swap_sibling_early")))
    sent_last, out_d["norm_g"], theirs = lax.optimization_barrier((sent_last, out_d["norm_g"], theirs))
    core_sum["w_in"] = lax.switch(me[0], [
        lambda: _sum_block(own["block0"], sent_last[0]),
        lambda: jnp.concatenate([_sum_block(own["head1"], sent_last[1]), _sum_block(own["tail1"], sent[0])], axis=1),
        lambda: _sum_block(own["block2"], sent[1]),
        lambda: _sum_block(own["block3"], sent[2])])
    core_sum.update({n: _sum4(own[n], r, me) for n, r in zip(loras, sent_last[2:])})
    rest = ("w_in",) + loras
    theirs.update(zip(rest, _swap_sibling([core_sum[n] for n in rest], "swap_sibling")))
    for n in SHARDED:
        g, d, m2, v2 = _adamw(w[n][0], m[n][0], v[n][0], [core_sum[n], theirs[n]], "adamw_" + n)
        out_g[n], out_d[n], out_m[n], out_v[n] = (a.reshape(shapes[n]) for a in (g, d, m2, v2))

    return (loss, grad_x.reshape(x.shape), *[out_g[n] for n in WEIGHTS], *[out_d[n] for n in WEIGHTS],
            *[out_m[n] for n in WEIGHTS], *[out_v[n] for n in WEIGHTS])
```

```python
import functools
import math

import jax
import jax.numpy as jnp
from jax import lax
from jax.experimental import pallas as pl
from jax.experimental.pallas import tpu as pltpu
from jax.experimental.pallas import tpu_sc as plsc

F32 = jnp.float32
BF16 = jnp.bfloat16

D_MODEL = 1024
D_HALF = 512
HEAD = 64
N_HEADS = 8
LORA = 64
RWKV_COLS = 2176
FOX_REAL = 2056
SEC = 2176
GATE_COLS = 2048
IN_COLS = 6280
N_CHIPS = 4
SHARD_COLS = IN_COLS // N_CHIPS
A_TAIL = RWKV_COLS - SHARD_COLS
B_HEAD = SHARD_COLS - A_TAIL
B_TAIL = FOX_REAL - B_HEAD
G_HEAD = SHARD_COLS - B_TAIL
RMS_EPS = 1e-6
LNX_EPS = 64e-5
ATT_SCALE = HEAD ** -0.5
NEG = -1e30

ADAM_LR = 0.001
ADAM_B1 = 0.9
ADAM_B2 = 0.999
ADAM_EPS = 1e-08
ADAM_WD = 0.01
ADAM_STEP = 10

LANES = 128
SUBLANES = 8
VMEM_LIMIT = 56 * 1024 * 1024
MESH = pl.DeviceIdType.MESH


def _params(*sem):
    return pltpu.CompilerParams(dimension_semantics=sem if sem else None, vmem_limit_bytes=VMEM_LIMIT)


def _sigmoid(x):
    return 1.0 / (1.0 + jnp.exp(-x))


def _log_sigmoid(x):
    return jnp.minimum(x, 0.0) - jnp.log(1.0 + jnp.exp(-jnp.abs(x)))


def _head_ones():
    r = lax.broadcasted_iota(jnp.int32, (LANES, LANES), 0) >> 6
    c = lax.broadcasted_iota(jnp.int32, (LANES, LANES), 1) >> 6
    return (r == c).astype(BF16)


def _split3(x):
    hi = x.astype(BF16)
    r1 = x - hi.astype(F32)
    mid = r1.astype(BF16)
    lo = (r1 - mid.astype(F32)).astype(BF16)
    return hi, mid, lo


def _exact_dot(x, ones_bf16, ones_first=False):
    out = None
    for piece in _split3(x):
        if ones_first:
            t = jnp.dot(ones_bf16, piece, preferred_element_type=F32)
        else:
            t = jnp.dot(piece, ones_bf16, preferred_element_type=F32)
        out = t if out is None else out + t
    return out


def _head_sum(x, bd):
    n = x.shape[1] // LANES
    parts = [_exact_dot(x[:, i * LANES:(i + 1) * LANES], bd) for i in range(n)]
    return parts[0] if n == 1 else jnp.concatenate(parts, axis=1)


def _dot_nt(a, b):
    return lax.dot_general(a, b, (((1,), (1,)), ((), ())), preferred_element_type=F32)


def _dot_tn(a, b):
    return lax.dot_general(a, b, (((0,), (0,)), ((), ())), preferred_element_type=F32)


def _colsum(x):
    return jnp.sum(x, axis=0, keepdims=True)


def _rmsnorm_in(x, g, tm=512):
    s, d = x.shape

    def body(x_ref, g_ref, h_ref):
        xv = x_ref[...]
        r = lax.rsqrt(jnp.mean(xv * xv, axis=-1, keepdims=True) + RMS_EPS)
        h_ref[...] = (xv * r * g_ref[...]).astype(BF16)

    return pl.pallas_call(
        body, name="rmsnorm_in", grid=(s // tm,),
        in_specs=[pl.BlockSpec((tm, d), lambda i: (i, 0)), pl.BlockSpec((1, d), lambda i: (0, 0))],
        out_specs=pl.BlockSpec((tm, d), lambda i: (i, 0)),
        out_shape=jax.ShapeDtypeStruct((s, d), BF16), compiler_params=_params("parallel"),
    )(x, g)


def _matmul_nn(a, b, name, tm=512):
    m, k = a.shape
    n = b.shape[1]

    def body(a_ref, b_ref, o_ref):
        o_ref[...] = jnp.dot(a_ref[...], b_ref[...], preferred_element_type=F32)

    return pl.pallas_call(
        body, name=name, grid=(m // tm,),
        in_specs=[pl.BlockSpec((tm, k), lambda i: (i, 0)), pl.BlockSpec((k, n), lambda i: (0, 0))],
        out_specs=pl.BlockSpec((tm, n), lambda i: (i, 0)),
        out_shape=jax.ShapeDtypeStruct((m, n), F32), compiler_params=_params("parallel"),
    )(a, b)


def _matmul_tn_acc(at, b, name, tk=512):
    m, k = at.shape
    n = b.shape[1]

    def body(a_ref, b_ref, o_ref):
        j = pl.program_id(0)

        @pl.when(j == 0)
        def _():
            o_ref[...] = jnp.zeros_like(o_ref)

        o_ref[...] += jnp.dot(a_ref[...], b_ref[...].astype(BF16), preferred_element_type=F32)

    return pl.pallas_call(
        body, name=name, grid=(k // tk,),
        in_specs=[pl.BlockSpec((m, tk), lambda j: (0, j)), pl.BlockSpec((tk, n), lambda j: (j, 0))],
        out_specs=pl.BlockSpec((m, n), lambda j: (0, 0)),
        out_shape=jax.ShapeDtypeStruct((m, n), F32), compiler_params=_params("arbitrary"),
    )(at, b)


def _inproj_bwd(du_a, du_b, du_g, w_a, w_b, w_g, x, dx2, g, exchange=None, tm=256):
    s, d = x.shape
    nb = s // tm

    def body(*refs):
        ((da_ref, db_ref, dg_ref, wa_ref, wb_ref, wg_ref, x_ref, dx2_ref, g_ref), (gx_ref, gg_ref), _,
         moves) = _split_refs(refs, 9, 2, exchange)
        i = pl.program_id(0)
        if moves:
            moves.start(also=(i == 0))

        @pl.when(i == 0)
        def _():
            gg_ref[...] = jnp.zeros_like(gg_ref)

        dh = _dot_nt(da_ref[...].astype(BF16), wa_ref[...])
        dh += _dot_nt(db_ref[...].astype(BF16), wb_ref[...])
        dh += _dot_nt(dg_ref[...].astype(BF16), wg_ref[...])
        xv = x_ref[...]
        r = lax.rsqrt(jnp.mean(xv * xv, axis=-1, keepdims=True) + RMS_EPS)
        xh = xv * r
        gg_ref[...] += _colsum(dh * xh)
        dxh = dh * g_ref[...]
        gx_ref[...] = dx2_ref[...] + r * (dxh - xh * jnp.mean(dxh * xh, axis=-1, keepdims=True))
        if moves:
            moves.wait(also=(i == nb - 1))

    row = lambda w: pl.BlockSpec((tm, w), lambda i: (i, 0))
    full = lambda a: pl.BlockSpec(a.shape, lambda i: (0, 0))
    ex_in = exchange.operands if exchange else []
    ex_out = exchange.out_shapes if exchange else []
    res = pl.pallas_call(
        body, name="inproj_bwd", grid=(nb,),
        in_specs=[row(SEC), row(SEC), row(GATE_COLS), full(w_a), full(w_b), full(w_g), row(d), row(d), full(g)]
                 + [ANY] * len(ex_in),
        out_specs=[row(d), pl.BlockSpec((1, d), lambda i: (0, 0))] + [ANY] * len(ex_out),
        out_shape=[jax.ShapeDtypeStruct((s, d), F32), jax.ShapeDtypeStruct((1, d), F32)] + ex_out,
        scratch_shapes=exchange.scratch() if exchange else [],
        compiler_params=_params("arbitrary"),
    )(du_a, du_b, du_g, w_a, w_b, w_g, x, dx2, g, *ex_in)
    return res[0], res[1], list(res[2:])


def _rwkv_elementwise(ua, prev_row, first, mu, wl, w0, a0, kkw, kaw, bd):
    tm = ua.shape[0]
    rows = lax.broadcasted_iota(jnp.int32, (tm, 1), 0)
    prev = jnp.where(first, jnp.zeros_like(prev_row), prev_row)
    shifted = jnp.where(rows == 0, prev, pltpu.roll(ua, 1, 0))
    delta = shifted - ua
    us = ua + delta * mu
    r = us[:, 0:512]
    k0 = us[:, 512:1024]
    v = us[:, 1024:1536]
    lo = us[:, 1536:1664]
    gate = us[:, 1664:2176]
    lane = lax.broadcasted_iota(jnp.int32, (1, LANES), 1)
    th = jnp.tanh(lo)
    lin = jnp.where(lane < LORA, th, lo)
    ll = jnp.dot(lin.astype(BF16), wl, preferred_element_type=F32)
    sz = _sigmoid(w0 + ll[:, :512])
    e = sz * math.exp(-0.5)
    dec = jnp.exp(-e)
    a = _sigmoid(a0 + ll[:, 512:])
    kk0 = k0 * kkw
    ss = _head_sum(kk0 * kk0, bd)
    nrm = jnp.maximum(jnp.sqrt(ss), 1e-12)
    kk = kk0 / nrm
    k = k0 * (1.0 + (a - 1.0) * kaw)
    return dict(delta=delta, us=us, r=r, k0=k0, v=v, lo=lo, gate=gate, th=th, lin=lin, sz=sz, e=e, dec=dec,
                a=a, kk0=kk0, ss=ss, nrm=nrm, kk=kk, k=k)


def _rwkv_prep(u_a, mu, wl, w0, a0, kkw, kaw, tm=256):
    s = u_a.shape[0]

    def body(ua_ref, prev_ref, mu_ref, wl_ref, w0_ref, a0_ref, kkw_ref, kaw_ref,
             r_ref, w_ref, k_ref, v_ref, a_ref, b_ref, g_ref):
        i = pl.program_id(0)
        f = _rwkv_elementwise(ua_ref[...], prev_ref[7:8, :], i == 0, mu_ref[...], wl_ref[...], w0_ref[...],
                              a0_ref[...], kkw_ref[...], kaw_ref[...], _head_ones())
        r_ref[...] = f["r"]
        w_ref[...] = f["dec"]
        k_ref[...] = f["k"]
        v_ref[...] = f["v"]
        a_ref[...] = -f["kk"]
        b_ref[...] = f["kk"] * f["a"]
        g_ref[...] = f["gate"]

    vec = lambda w: pl.BlockSpec((1, w), lambda i: (0, 0))
    out = pl.BlockSpec((tm, D_HALF), lambda i: (i, 0))
    return pl.pallas_call(
        body, name="rwkv_prep", grid=(s // tm,),
        in_specs=[pl.BlockSpec((tm, SEC), lambda i: (i, 0)),
                  pl.BlockSpec((8, SEC), lambda i: (jnp.maximum(i * (tm // 8) - 1, 0), 0)),
                  vec(SEC), pl.BlockSpec((LANES, 2 * D_HALF), lambda i: (0, 0)),
                  vec(D_HALF), vec(D_HALF), vec(D_HALF), vec(D_HALF)],
        out_specs=[out] * 7,
        out_shape=[jax.ShapeDtypeStruct((s, D_HALF), F32)] * 7,
        compiler_params=_params("parallel"),
    )(u_a, u_a, mu, wl, w0, a0, kkw, kaw)


SCAN_TB = 128
N_PAIRS = 4


def _pair_sum(x, left):
    s_l = jnp.sum(jnp.where(left, x, 0.0), axis=1, keepdims=True)
    s_r = jnp.sum(jnp.where(left, 0.0, x), axis=1, keepdims=True)
    return jnp.where(left, s_l, s_r)


def _pair_dot(x, row_l, row_r, left):
    s_l = jnp.sum(x * row_l, axis=1, keepdims=True)
    s_r = jnp.sum(x * row_r, axis=1, keepdims=True)
    return jnp.where(left, s_l, s_r)


def _halves(rows8):
    lane = lax.broadcasted_iota(jnp.int32, rows8.shape, 1)
    keep_left = (lane & (LANES - 1)) < HEAD
    return jnp.where(keep_left, rows8, 0.0), jnp.where(keep_left, 0.0, rows8)


def _quad_consts():
    lane = lax.broadcasted_iota(jnp.int32, (HEAD, 2 * LANES), 1)
    rowi = lax.broadcasted_iota(jnp.int32, (HEAD, 2 * LANES), 0)
    diag2 = rowi == (lane & (HEAD - 1))
    r = lax.broadcasted_iota(jnp.int32, (2 * LANES, 2 * LANES), 0) >> 6
    c = lax.broadcasted_iota(jnp.int32, (2 * LANES, 2 * LANES), 1) >> 6
    return diag2, (r == c).astype(BF16)


def _rows_to_columns(x8, diag2, bd2):
    lhs = jnp.concatenate([jnp.where(diag2, x8[i:i + 1], 0.0).astype(BF16) for i in range(SUBLANES)], axis=0)
    return jnp.dot(lhs, bd2, preferred_element_type=F32)


def _diag_rows(qtile, diag2, bd2, sub_row2):
    res = jnp.dot(qtile, bd2, preferred_element_type=F32)
    out = jnp.zeros((SUBLANES, 2 * LANES), F32)
    for i in range(SUBLANES):
        out = jnp.where(sub_row2 == i, _colsum(jnp.where(diag2, res[i * HEAD:(i + 1) * HEAD], 0.0)), out)
    return out


def _store_tile(qbuf, slot, p, i, x):
    qbuf[slot, p // 2, i * HEAD:(i + 1) * HEAD, (p % 2) * LANES:(p % 2 + 1) * LANES] = x.astype(BF16)


def _left_half():
    return lax.broadcasted_iota(jnp.int32, (HEAD, LANES), 1) < HEAD


def _split_refs(refs, n_rows, n_out, exchange):
    n_in = len(exchange.operands) if exchange else 0
    n_ex_out = len(exchange.out_shapes) if exchange else 0
    refs = list(refs)
    rows, refs = refs[:n_rows], refs[n_rows:]
    ex_in, refs = refs[:n_in], refs[n_in:]
    outs, refs = refs[:n_out], refs[n_out:]
    ex_out, refs = refs[:n_ex_out], refs[n_ex_out:]
    scratch, sems = (refs[:-3], refs[-3:]) if exchange else (refs, None)
    moves = exchange.moves(ex_in, ex_out, sems) if exchange else None
    return rows, outs, scratch, moves


def _wkv_fwd(r, w, k, a, b, v, exchange=None):
    s = r.shape[0]
    tb = SCAN_TB
    nb = s // tb

    def body(*refs):
        (r_ref, w_ref, k_ref, a_ref, b_ref, v_ref), (y_ref, st_ref), (state, vbuf, qbuf), moves = _split_refs(
            refs, 6, 2, exchange)
        g = pl.program_id(0)
        if moves:
            moves.start(also=(g == 0))

        @pl.when(g == 0)
        def _():
            state[...] = jnp.zeros_like(state)
            qbuf[...] = jnp.zeros_like(qbuf)

        left = _left_half()
        diag2, bd2 = _quad_consts()
        sub_row2 = lax.broadcasted_iota(jnp.int32, (SUBLANES, 2 * LANES), 0)
        groups = tb // SUBLANES
        quads = [slice(g2 * 2 * LANES, (g2 + 1) * 2 * LANES) for g2 in range(2)]

        def rows_of(q):
            return pl.ds(pl.multiple_of(q * SUBLANES, SUBLANES), SUBLANES)

        def v_tiles(q, slot):
            v8 = v_ref[rows_of(q), :]
            for g2 in range(2):
                vbuf[slot, g2] = _rows_to_columns(v8[:, quads[g2]], diag2, bd2)

        def chain(q, slot):
            rows8 = rows_of(q)
            a8, w8, b8, k8, r8 = (x[rows8, :] for x in (a_ref, w_ref, b_ref, k_ref, r_ref))
            pairs = [slice(p * LANES, (p + 1) * LANES) for p in range(N_PAIRS)]
            a_next = pltpu.roll(a8, SUBLANES - 1, 0)
            (a8_l, a8_r), (wa8_l, wa8_r) = _halves(a8), _halves(w8 * a_next)
            ba8 =jnp.concatenate([_pair_sum(b8[:, pr] * a_next[:, pr], left[0:SUBLANES]) for pr in pairs], axis=1)
            ka8 = jnp.concatenate([_pair_sum(k8[:, pr] * a_next[:, pr], left[0:SUBLANES]) for pr in pairs], axis=1)
            sp = [state[p] for p in range(N_PAIRS)]
            for i in range(0, SUBLANES, 2):
                r0, r1 = slice(i, i + 1), slice(i + 1, i + 2)
                sums = [(_pair_dot(sp[p], a8_l[r0, pairs[p]], a8_r[r0, pairs[p]], left),
                         _pair_dot(sp[p], wa8_l[r0, pairs[p]], wa8_r[r0, pairs[p]], left)) for p in range(N_PAIRS)]
                sa0, sa1 = [s[0] for s in sums], [s[1] for s in sums]
                for p in range(N_PAIRS):
                    pr = pairs[p]
                    inner = slice((p % 2) * LANES, (p % 2 + 1) * LANES)
                    vt0 = vbuf[slot, p // 2, i * HEAD:(i + 1) * HEAD, inner]
                    vt1 = vbuf[slot, p // 2, (i + 1) * HEAD:(i + 2) * HEAD, inner]
                    sa_next = sa1[p] + sa0[p] * ba8[r0, pr] + vt0 * ka8[r0, pr]
                    s1 = sp[p] * w8[r0, pr] + sa0[p] * b8[r0, pr] + vt0 * k8[r0, pr]
                    st_ref[q * SUBLANES + i, p] = s1
                    _store_tile(qbuf, slot, p, i, s1 * r8[r0, pr])
                    s2 = s1 * w8[r1, pr] + sa_next * b8[r1, pr] + vt1 * k8[r1, pr]
                    st_ref[q * SUBLANES + i + 1, p] = s2
                    _store_tile(qbuf, slot, p, i + 1, s2 * r8[r1, pr])
                    sp[p] = s2
            for p in range(N_PAIRS):
                state[p] = sp[p]

        def y_rows(q, slot):
            for g2 in range(2):
                y_ref[rows_of(q), quads[g2]] = _diag_rows(qbuf[slot, g2], diag2, bd2, sub_row2)

        v_tiles(0, 0)

        def two_groups(j, carry):
            q0 = 2 * j
            v_tiles(q0 + 1, 1)
            chain(q0, 0)
            y_rows(jnp.maximum(q0 - 1, 0), 1)
            v_tiles(jnp.minimum(q0 + 2, groups - 1), 0)
            chain(q0 + 1, 1)
            y_rows(q0, 0)
            return carry

        lax.fori_loop(0, groups // 2, two_groups, 0)
        y_rows(groups - 1, 1)
        if moves:
            moves.wait(also=(g == nb - 1))

    rows = pl.BlockSpec((tb, D_HALF), lambda g: (g, 0))
    ex_in = exchange.operands if exchange else []
    ex_out = exchange.out_shapes if exchange else []
    res = pl.pallas_call(
        body, name="wkv_fwd", grid=(nb,),
        in_specs=[rows] * 6 + [ANY] * len(ex_in),
        out_specs=[rows, pl.BlockSpec((tb, N_PAIRS, HEAD, LANES), lambda g: (g, 0, 0, 0))] + [ANY] * len(ex_out),
        out_shape=[jax.ShapeDtypeStruct((s, D_HALF), F32),
                   jax.ShapeDtypeStruct((s, N_PAIRS, HEAD, LANES), F32)] + ex_out,
        scratch_shapes=[pltpu.VMEM((N_PAIRS, HEAD, LANES), F32),
                        pltpu.VMEM((2, 2, SUBLANES * HEAD, 2 * LANES), F32),
                        pltpu.VMEM((2, 2, SUBLANES * HEAD, 2 * LANES), BF16)]
                       + (exchange.scratch() if exchange else []),
        compiler_params=_params("arbitrary"),
    )(r, w, k, a, b, v, *ex_in)
    return res[0], res[1], list(res[2:])


def _wkv_bwd(r, w, k, a, b, v, dy, st, exchange=None):
    s = r.shape[0]
    tb = SCAN_TB
    nb = s // tb

    def body(*refs):
        ((r_ref, w_ref, k_ref, a_ref, b_ref, v_ref, dy_ref, st_ref, before_ref),
         (dr_ref, dw_ref, dk_ref, dv_ref, da_ref, db_ref), (dstate, vbuf, qbuf, sbuf),
         moves) = _split_refs(refs, 9, 6, exchange)
        g = pl.program_id(0)
        first_block = g == nb - 1
        if moves:
            moves.start(also=(g == 0))

        @pl.when(g == 0)
        def _():
            dstate[...] = jnp.zeros_like(dstate)
            qbuf[...] = jnp.zeros_like(qbuf)

        left = _left_half()
        diag2, bd2 = _quad_consts()
        sub_row = lax.broadcasted_iota(jnp.int32, (SUBLANES, LANES), 0)
        sub_row2 = lax.broadcasted_iota(jnp.int32, (SUBLANES, 2 * LANES), 0)
        groups = tb // SUBLANES
        quads = [slice(g2 * 2 * LANES, (g2 + 1) * 2 * LANES) for g2 in range(2)]
        row_refs = (dr_ref, dw_ref, dk_ref, da_ref, db_ref)

        def rows_of(q):
            return pl.ds(pl.multiple_of(q * SUBLANES, SUBLANES), SUBLANES)

        def state_before(q, i, p):
            if i > 0:
                return st_ref[q * SUBLANES + i - 1, p]
            return jnp.where(q == 0, jnp.where(first_block, 0.0, before_ref[0, p]),
                             st_ref[jnp.maximum(q * SUBLANES - 1, 0), p])

        def column_tiles(q, slot):
            rows8 = rows_of(q)
            for kind, ref in enumerate((v_ref, dy_ref)):
                x8 = ref[rows8, :]
                for g2 in range(2):
                    vbuf[slot, kind, g2] = _rows_to_columns(x8[:, quads[g2]], diag2, bd2)
            a8 = a_ref[rows8, :]
            for i in range(SUBLANES):
                for p in range(N_PAIRS):
                    _store_tile(sbuf, 0, p, i, state_before(q, i, p) * a8[i:i + 1, p * LANES:(p + 1) * LANES])
            for g2 in range(2):
                vbuf[slot, 2, g2] = jnp.dot(sbuf[0, g2], bd2, preferred_element_type=F32)

        def chain(q, slot):
            rows8 = rows_of(q)
            a8, w8, b8, k8, r8 = (x[rows8, :] for x in (a_ref, w_ref, b_ref, k_ref, r_ref))
            b8_l, b8_r = _halves(b8)
            dsp = [dstate[p] for p in range(N_PAIRS)]
            outs = [[jnp.zeros((SUBLANES, LANES), F32) for _ in row_refs] for _ in range(N_PAIRS)]
            after = [st_ref[q * SUBLANES + SUBLANES - 1, p] for p in range(N_PAIRS)]
            for i in reversed(range(SUBLANES)):
                row = slice(i, i + 1)
                pl_ = [slice(p * LANES, (p + 1) * LANES) for p in range(N_PAIRS)]
                tile = [(p // 2, slice(i * HEAD, (i + 1) * HEAD), slice((p % 2) * LANES, (p % 2 + 1) * LANES))
                        for p in range(N_PAIRS)]
                sp = [state_before(q, i, p) for p in range(N_PAIRS)]
                dyt = [vbuf[(slot, 1) + tile[p]] for p in range(N_PAIRS)]
                ds = [dsp[p] + dyt[p] * r8[row, pl_[p]] for p in range(N_PAIRS)]
                dsa = [_pair_dot(ds[p], b8_l[row, pl_[p]], b8_r[row, pl_[p]], left) for p in range(N_PAIRS)]
                sa = [vbuf[(slot, 2) + tile[p]] for p in range(N_PAIRS)]
                for p in range(N_PAIRS):
                    ar, wr, br, kr = (x[row, pl_[p]] for x in (a8, w8, b8, k8))
                    vt = vbuf[(slot, 0) + tile[p]]
                    dsp[p] = ds[p] * wr + dsa[p] * ar
                    new = (_colsum(after[p] * dyt[p]), _colsum(ds[p] * sp[p]), _colsum(ds[p] * vt),
                           _colsum(sp[p] * dsa[p]), _colsum(ds[p] * sa[p]))
                    outs[p] = [jnp.where(sub_row == i, n, o) for n, o in zip(new, outs[p])]
                    _store_tile(qbuf, slot, p, i, ds[p] * kr)
                after = sp
            for p in range(N_PAIRS):
                dstate[p] = dsp[p]
                for ref, o in zip(row_refs, outs[p]):
                    ref[rows8, p * LANES:(p + 1) * LANES] = o

        def dv_rows(q, slot):
            for g2 in range(2):
                dv_ref[rows_of(q), quads[g2]] = _diag_rows(qbuf[slot, g2], diag2, bd2, sub_row2)

        column_tiles(groups - 1, 0)

        def two_groups(j, carry):
            q0 = groups - 1 - 2 * j
            column_tiles(q0 - 1, 1)
            chain(q0, 0)
            dv_rows(jnp.minimum(q0 + 1, groups - 1), 1)
            column_tiles(jnp.maximum(q0 - 2, 0), 0)
            chain(q0 - 1, 1)
            dv_rows(q0, 0)
            return carry

        lax.fori_loop(0, groups // 2, two_groups, 0)
        dv_rows(0, 1)
        if moves:
            moves.wait(also=(g == nb - 1))

    rows = pl.BlockSpec((tb, D_HALF), lambda g: (nb - 1 - g, 0))
    ex_in = exchange.operands if exchange else []
    ex_out = exchange.out_shapes if exchange else []
    res = pl.pallas_call(
        body, name="wkv_bwd", grid=(nb,),
        in_specs=[rows] * 7 + [pl.BlockSpec((tb, N_PAIRS, HEAD, LANES), lambda g: (nb - 1 - g, 0, 0, 0)),
                               pl.BlockSpec((1, N_PAIRS, HEAD, LANES),
                                            lambda g: (jnp.maximum((nb - 1 - g) * tb - 1, 0), 0, 0, 0))]
                 + [ANY] * len(ex_in),
        out_specs=[rows] * 6 + [ANY] * len(ex_out),
        out_shape=[jax.ShapeDtypeStruct((s, D_HALF), F32)] * 6 + ex_out,
        scratch_shapes=[pltpu.VMEM((N_PAIRS, HEAD, LANES), F32),
                        pltpu.VMEM((2, 3, 2, SUBLANES * HEAD, 2 * LANES), F32),
                        pltpu.VMEM((2, 2, SUBLANES * HEAD, 2 * LANES), BF16),
                        pltpu.VMEM((1, 2, SUBLANES * HEAD, 2 * LANES), BF16)]
                       + (exchange.scratch() if exchange else []),
        compiler_params=_params("arbitrary"),
    )(r, w, k, a, b, v, dy, st, st, *ex_in)
    return list(res[:6]), list(res[6:])


def _rwkv_post_math(y, r, k, v, gate, lw, lb, rk, bd):
    mean = _head_sum(y, bd) * (1.0 / HEAD)
    yc = y - mean
    var = _head_sum(yc * yc, bd) * (1.0 / HEAD)
    rstd = lax.rsqrt(var + LNX_EPS)
    yn = yc * rstd
    rkk = _head_sum(r * k * rk, bd)
    sg = _sigmoid(gate)
    pre = yn * lw + lb + rkk * v
    return yn, rstd, rkk, sg, pre


def _rwkv_post(y, r, k, v, gate, lw, lb, rk, tm=256):
    s = y.shape[0]

    def body(y_ref, r_ref, k_ref, v_ref, g_ref, lw_ref, lb_ref, rk_ref, o_ref):
        gate_v = g_ref[...]
        _, _, _, sg, pre = _rwkv_post_math(y_ref[...], r_ref[...], k_ref[...], v_ref[...], gate_v,
                                           lw_ref[...], lb_ref[...], rk_ref[...], _head_ones())
        o_ref[...] = pre * (gate_v * sg)

    blk = pl.BlockSpec((tm, D_HALF), lambda i: (i, 0))
    vec = pl.BlockSpec((1, D_HALF), lambda i: (0, 0))
    return pl.pallas_call(
        body, name="rwkv_post", grid=(s // tm,),
        in_specs=[blk] * 5 + [vec] * 3, out_specs=blk,
        out_shape=jax.ShapeDtypeStruct((s, D_HALF), F32), compiler_params=_params("parallel"),
    )(y, r, k, v, gate, lw, lb, rk)


def _rwkv_post_bwd(dmix, y, r, k, v, gate, lw, lb, rk, tm=256):
    s = y.shape[0]

    def body(dm_ref, y_ref, r_ref, k_ref, v_ref, g_ref, lw_ref, lb_ref, rk_ref,
             dy_ref, dr_ref, dk_ref, dv_ref, dg_ref, dlw_ref, dlb_ref, drk_ref):
        i = pl.program_id(0)

        @pl.when(i == 0)
        def _():
            dlw_ref[...] = jnp.zeros_like(dlw_ref)
            dlb_ref[...] = jnp.zeros_like(dlb_ref)
            drk_ref[...] = jnp.zeros_like(drk_ref)

        bd = _head_ones()
        rv, kv, vv, gate_v, lw_v, rk_v = r_ref[...], k_ref[...], v_ref[...], g_ref[...], lw_ref[...], rk_ref[...]
        yn, rstd, rkk, sg, pre = _rwkv_post_math(y_ref[...], rv, kv, vv, gate_v, lw_v, lb_ref[...], rk_v, bd)
        dm = dm_ref[...]
        dg_ref[...] = dm * pre * (sg * (1.0 + gate_v * (1.0 - sg)))
        dpre = dm * (gate_v * sg)
        dlw_ref[...] += _colsum(dpre * yn)
        dlb_ref[...] += _colsum(dpre)
        dyn = dpre * lw_v
        m1 = _head_sum(dyn, bd) * (1.0 / HEAD)
        m2 = _head_sum(dyn * yn, bd) * (1.0 / HEAD)
        dy_ref[...] = rstd * (dyn - m1 - yn * m2)
        dv_ref[...] = dpre * rkk
        drkk = _head_sum(dpre * vv, bd)
        dr_ref[...] = drkk * kv * rk_v
        dk_ref[...] = drkk * rv * rk_v
        drk_ref[...] += _colsum(drkk * rv * kv)

    blk = pl.BlockSpec((tm, D_HALF), lambda i: (i, 0))
    vec = pl.BlockSpec((1, D_HALF), lambda i: (0, 0))
    return pl.pallas_call(
        body, name="rwkv_post_bwd", grid=(s // tm,),
        in_specs=[blk] * 6 + [vec] * 3, out_specs=[blk] * 5 + [vec] * 3,
        out_shape=[jax.ShapeDtypeStruct((s, D_HALF), F32)] * 5 + [jax.ShapeDtypeStruct((1, D_HALF), F32)] * 3,
        compiler_params=_params("arbitrary"),
    )(dmix, y, r, k, v, gate, lw, lb, rk)


def _rwkv_prep_bwd(u_a, grads, mu, wl, w0, a0, kkw, kaw, tm=256):
    s = u_a.shape[0]
    nb = s // tm

    def body(ua_ref, prev_ref, drs_ref, dws_ref, dks_ref, dvs_ref, das_ref, dbs_ref, drb_ref, dkb_ref, dvb_ref,
             dgt_ref, mu_ref, wl_ref, w0_ref, a0_ref, kkw_ref, kaw_ref,
             du_ref, dmu_ref, dwl_ref, dw0_ref, da0_ref, dkkw_ref, dkaw_ref, carry):
        i = pl.program_id(0)

        @pl.when(i == 0)
        def _():
            carry[...] = jnp.zeros_like(carry)
            for ref in (dmu_ref, dwl_ref, dw0_ref, da0_ref, dkkw_ref, dkaw_ref):
                ref[...] = jnp.zeros_like(ref)

        bd = _head_ones()
        mu_v, wl_v, kkw_v, kaw_v = mu_ref[...], wl_ref[...], kkw_ref[...], kaw_ref[...]
        f = _rwkv_elementwise(ua_ref[...], prev_ref[7:8, :], i == nb - 1, mu_v, wl_v, w0_ref[...],
                              a0_ref[...], kkw_v, kaw_v, bd)
        a, kk, k0 = f["a"], f["kk"], f["k0"]
        dk = dks_ref[...] + dkb_ref[...]
        dbs = dbs_ref[...]
        dkk = dbs * a - das_ref[...]
        da = dbs * kk + dk * k0 * kaw_v
        dk0 = dk * (1.0 + (a - 1.0) * kaw_v)
        dkaw_ref[...] += _colsum(dk * k0 * (a - 1.0))
        inv = 1.0 / f["nrm"]
        proj = _head_sum(dkk * kk, bd)
        dkk0 = jnp.where(f["ss"] > 1e-24, (dkk - kk * proj) * inv, dkk * inv)
        dk0 = dk0 + dkk0 * kkw_v
        dkkw_ref[...] += _colsum(dkk0 * k0)
        dza = da * a * (1.0 - a)
        da0_ref[...] += _colsum(dza)
        dz = -dws_ref[...] * f["dec"] * f["e"] * (1.0 - f["sz"])
        dw0_ref[...] += _colsum(dz)
        dll = jnp.concatenate([dz, dza], axis=1).astype(BF16)
        dwl_ref[...] += _dot_tn(f["lin"].astype(BF16), dll)
        dlin = _dot_nt(dll, wl_v)
        lane = lax.broadcasted_iota(jnp.int32, (1, LANES), 1)
        th = f["th"]
        dlo = jnp.where(lane < LORA, dlin * (1.0 - th * th), dlin)
        dus = jnp.concatenate([drs_ref[...] + drb_ref[...], dk0, dvs_ref[...] + dvb_ref[...], dlo, dgt_ref[...]],
                              axis=1)
        dmu_ref[...] += _colsum(dus * f["delta"])
        g1 = dus * mu_v
        rows = lax.broadcasted_iota(jnp.int32, (tm, 1), 0)
        up = jnp.where(rows == tm - 1, carry[...], pltpu.roll(g1, tm - 1, 0))
        du_ref[...] = dus - g1 + up
        carry[...] = g1[0:1, :]

    rev = lambda w: pl.BlockSpec((tm, w), lambda i: (nb - 1 - i, 0))
    vec = lambda w: pl.BlockSpec((1, w), lambda i: (0, 0))
    wl_spec = pl.BlockSpec((LANES, 2 * D_HALF), lambda i: (0, 0))
    return pl.pallas_call(
        body, name="rwkv_prep_bwd", grid=(nb,),
        in_specs=[rev(SEC), pl.BlockSpec((8, SEC), lambda i: (jnp.maximum((nb - 1 - i) * (tm // 8) - 1, 0), 0))]
                 + [rev(D_HALF)] * 10 + [vec(SEC), wl_spec] + [vec(D_HALF)] * 4,
        out_specs=[rev(SEC), vec(SEC), wl_spec] + [vec(D_HALF)] * 4,
        out_shape=[jax.ShapeDtypeStruct((s, SEC), F32), jax.ShapeDtypeStruct((1, SEC), F32),
                   jax.ShapeDtypeStruct((LANES, 2 * D_HALF), F32)] + [jax.ShapeDtypeStruct((1, D_HALF), F32)] * 4,
        scratch_shapes=[pltpu.VMEM((1, SEC), F32)],
        compiler_params=_params("arbitrary"),
    )(u_a, u_a, *grads, mu, wl, w0, a0, kkw, kaw)


def _tri(tm, lower):
    r = lax.broadcasted_iota(jnp.int32, (tm, tm), 0)
    c = lax.broadcasted_iota(jnp.int32, (tm, tm), 1)
    return ((r >= c) if lower else (r <= c)).astype(BF16)


def _head_rms(x, g, bd):
    rinv = lax.rsqrt(_head_sum(x * x, bd) * (1.0 / HEAD) + RMS_EPS)
    xh = x * rinv
    return xh, rinv, xh * g


def _fox_prep(u_b, fb, qg, kg, tm=256):
    s = u_b.shape[0]

    def body(ub_ref, fb_ref, qg_ref, kg_ref, q_ref, k_ref, v_ref, cc_ref, cr_ref, carry):
        i = pl.program_id(0)

        @pl.when(i == 0)
        def _():
            carry[...] = jnp.zeros_like(carry)

        bd = _head_ones()
        _, _, qn = _head_rms(ub_ref[:, 0:512], qg_ref[...], bd)
        _, _, kn = _head_rms(ub_ref[:, 512:1024], kg_ref[...], bd)
        q_ref[...] = (qn * ATT_SCALE).astype(BF16)
        k_ref[...] = kn.astype(BF16)
        v_ref[...] = ub_ref[:, 1024:1536].astype(BF16)
        lane = lax.broadcasted_iota(jnp.int32, (1, LANES), 1)
        logf = jnp.where(lane < N_HEADS, _log_sigmoid(ub_ref[:, 2048:2176] + fb_ref[...]), 0.0)
        cum = _exact_dot(logf, _tri(tm, True), ones_first=True) + carry[...]
        for h in range(N_HEADS):
            cc_ref[h] = jnp.broadcast_to(cum[:, h:h + 1], (tm, LANES))
        cr_ref[...] = jnp.transpose(cum)[0:N_HEADS, :]
        carry[...] = cum[tm - 1:tm, :]

    blk = pl.BlockSpec((tm, D_HALF), lambda i: (i, 0))
    return pl.pallas_call(
        body, name="fox_prep", grid=(s // tm,),
        in_specs=[pl.BlockSpec((tm, SEC), lambda i: (i, 0)), pl.BlockSpec((1, LANES), lambda i: (0, 0)),
                  pl.BlockSpec((1, D_HALF), lambda i: (0, 0)), pl.BlockSpec((1, D_HALF), lambda i: (0, 0))],
        out_specs=[blk, blk, blk, pl.BlockSpec((N_HEADS, tm, LANES), lambda i: (0, i, 0)),
                   pl.BlockSpec((N_HEADS, tm), lambda i: (0, i))],
        out_shape=[jax.ShapeDtypeStruct((s, D_HALF), BF16)] * 3
                  + [jax.ShapeDtypeStruct((N_HEADS, s, LANES), F32), jax.ShapeDtypeStruct((N_HEADS, s), F32)],
        scratch_shapes=[pltpu.VMEM((1, LANES), F32)],
        compiler_params=_params("arbitrary"),
    )(u_b, fb, qg, kg)


ATT_T = 256


def _attn_fwd(q, k, v, cc, cr, u_b):
    s = q.shape[0]
    t = ATT_T
    nblk = s // t

    def body(q_ref, k_ref, v_ref, cc_ref, cr_ref, g_ref, o_ref, mix_ref, lse_ref, m_sc, l_sc, acc_sc):
        i = pl.program_id(0)
        j = pl.program_id(1)

        @pl.when(j == 0)
        def _():
            m_sc[...] = jnp.full_like(m_sc, NEG)
            l_sc[...] = jnp.zeros_like(l_sc)
            acc_sc[...] = jnp.zeros_like(acc_sc)

        def tile(on_diagonal):
            causal = _causal_tile(t) if on_diagonal else None
            left = lax.broadcasted_iota(jnp.int32, (1, LANES), 1) < HEAD
            for p in range(N_PAIRS):
                lanes = slice(p * LANES, (p + 1) * LANES)
                q2, k2, v2 = q_ref[:, lanes], k_ref[:, lanes], v_ref[:, lanes]
                acc2 = acc_sc[:, lanes]
                for e in range(2):
                    h = 2 * p + e
                    msk = left if e == 0 else jnp.logical_not(left)
                    sc = _dot_nt(jnp.where(msk, q2, jnp.zeros_like(q2)), k2)
                    sc = sc + (_wide(cc_ref[h]) - cr_ref[h:h + 1, :])
                    if on_diagonal:
                        sc = jnp.where(causal, sc, NEG)
                    m_prev = m_sc[h]
                    m_new = jnp.maximum(m_prev, jnp.max(sc, axis=1, keepdims=True))
                    alpha = jnp.exp(m_prev - m_new)
                    pm = jnp.exp(sc - _wide(m_new))
                    l_sc[h] = alpha * l_sc[h] + jnp.sum(pm, axis=1, keepdims=True)
                    m_sc[h] = m_new
                    pv = jnp.dot(pm.astype(BF16), v2, preferred_element_type=F32)
                    acc2 = jnp.where(msk, alpha * acc2 + pv, acc2)
                acc_sc[:, lanes] = acc2

        pl.when(j < i)(functools.partial(tile, False))
        pl.when(j == i)(functools.partial(tile, True))

        @pl.when(j == i)
        def _():
            left = lax.broadcasted_iota(jnp.int32, (1, LANES), 1) < HEAD
            for p in range(N_PAIRS):
                lanes = slice(p * LANES, (p + 1) * LANES)
                inv = jnp.where(left, 1.0 / l_sc[2 * p], 1.0 / l_sc[2 * p + 1])
                o = acc_sc[:, lanes] * inv
                o_ref[:, lanes] = o
                gate = g_ref[:, lanes]
                mix_ref[:, lanes] = o * (gate * _sigmoid(gate))
            for h in range(N_HEADS):
                lse_ref[h] = m_sc[h] + jnp.log(l_sc[h])

    qblk = pl.BlockSpec((t, D_HALF), lambda i, j: (i, 0))
    kblk = pl.BlockSpec((t, D_HALF), lambda i, j: (jnp.minimum(i, j), 0))
    return pl.pallas_call(
        body, name="fox_attn_fwd", grid=(nblk, nblk),
        in_specs=[qblk, kblk, kblk, pl.BlockSpec((N_HEADS, t, LANES), lambda i, j: (0, i, 0)),
                  pl.BlockSpec((N_HEADS, t), lambda i, j: (0, jnp.minimum(i, j))),
                  pl.BlockSpec((t, D_HALF), lambda i, j: (i, 3))],
        out_specs=[qblk, qblk, pl.BlockSpec((N_HEADS, t, LANES), lambda i, j: (0, i, 0))],
        out_shape=[jax.ShapeDtypeStruct((s, D_HALF), F32), jax.ShapeDtypeStruct((s, D_HALF), F32),
                   jax.ShapeDtypeStruct((N_HEADS, s, LANES), F32)],
        scratch_shapes=[pltpu.VMEM((N_HEADS, t, LANES), F32), pltpu.VMEM((N_HEADS, t, LANES), F32),
                        pltpu.VMEM((t, D_HALF), F32)],
        compiler_params=_params("parallel", "arbitrary"),
    )(q, k, v, cc, cr, u_b)


def _fox_post_bwd(dmix, o, u_b, tm=256):
    s = o.shape[0]

    def body(dm_ref, o_ref, g_ref, do_ref, dg_ref):
        gate = g_ref[...]
        sg = _sigmoid(gate)
        dm = dm_ref[...]
        do_ref[...] = (dm * (gate * sg)).astype(BF16)
        dg_ref[...] = dm * o_ref[...] * (sg * (1.0 + gate * (1.0 - sg)))

    blk = pl.BlockSpec((tm, D_HALF), lambda i: (i, 0))
    return pl.pallas_call(
        body, name="fox_post_bwd", grid=(s // tm,),
        in_specs=[blk, blk, pl.BlockSpec((tm, D_HALF), lambda i: (i, 3))], out_specs=[blk] * 2,
        out_shape=[jax.ShapeDtypeStruct((s, D_HALF), BF16), jax.ShapeDtypeStruct((s, D_HALF), F32)],
        compiler_params=_params("parallel"),
    )(dmix, o, u_b)


def _causal_tile(t):
    return lax.broadcasted_iota(jnp.int32, (t, t), 0) >= lax.broadcasted_iota(jnp.int32, (t, t), 1)


def _wide(x):
    return jnp.concatenate([x, x], axis=1)


def _attn_probs(q2, k2, v2, do2, msk, causal, bias, lse_rows):
    zero = jnp.zeros_like(q2)
    qh = jnp.where(msk, q2, zero)
    doh = jnp.where(msk, do2, zero)
    sc = _dot_nt(qh, k2) + bias
    if causal is not None:
        sc = jnp.where(causal, sc, NEG)
    pm = jnp.exp(sc - _wide(lse_rows))
    dp = _dot_nt(doh, v2)
    return qh, doh, pm, dp


def _attn_bwd_rowdot(q, k, v, do, lse, cc, cr):
    s = q.shape[0]
    t = ATT_T
    nblk = s // t

    def body(q_ref, k_ref, v_ref, do_ref, lse_ref, cc_ref, cr_ref, dd_ref, acc):
        i = pl.program_id(0)
        j = pl.program_id(1)

        @pl.when(j == 0)
        def _():
            acc[...] = jnp.zeros_like(acc)

        def tile(on_diagonal):
            causal = _causal_tile(t) if on_diagonal else None
            left = lax.broadcasted_iota(jnp.int32, (1, LANES), 1) < HEAD
            for p in range(N_PAIRS):
                lanes = slice(p * LANES, (p + 1) * LANES)
                q2, k2, v2, do2 = q_ref[:, lanes], k_ref[:, lanes], v_ref[:, lanes], do_ref[:, lanes]
                for e in range(2):
                    h = 2 * p + e
                    msk = left if e == 0 else jnp.logical_not(left)
                    bias = _wide(cc_ref[h]) - cr_ref[h:h + 1, :]
                    _, _, pm, dp = _attn_probs(q2, k2, v2, do2, msk, causal, bias, lse_ref[h])
                    acc[h] += jnp.sum(pm * dp, axis=1, keepdims=True)

        pl.when(j < i)(functools.partial(tile, False))
        pl.when(j == i)(functools.partial(tile, True))

        @pl.when(j == i)
        def _():
            dd_ref[...] = acc[...]

    qblk = pl.BlockSpec((t, D_HALF), lambda i, j: (i, 0))
    qcol = pl.BlockSpec((N_HEADS, t, LANES), lambda i, j: (0, i, 0))
    kblk = pl.BlockSpec((t, D_HALF), lambda i, j: (jnp.minimum(i, j), 0))
    return pl.pallas_call(
        body, name="fox_attn_rowdot", grid=(nblk, nblk),
        in_specs=[qblk, kblk, kblk, qblk, qcol, qcol, pl.BlockSpec((N_HEADS, t), lambda i, j: (0, jnp.minimum(i, j)))],
        out_specs=qcol, out_shape=jax.ShapeDtypeStruct((N_HEADS, s, LANES), F32),
        scratch_shapes=[pltpu.VMEM((N_HEADS, t, LANES), F32)],
        compiler_params=_params("parallel", "arbitrary"),
    )(q, k, v, do, lse, cc, cr)


def _attn_bwd(q, k, v, do, lse, dd, cc, cr):
    s = q.shape[0]
    t = ATT_T
    nblk = s // t

    def body(q_ref, k_ref, v_ref, do_ref, lse_ref, dd_ref, cc_ref, cr_ref,
             dq_ref, dk_ref, dv_ref, dcr_ref, dk_sc, dv_sc, dcr_sc):
        j = pl.program_id(0)
        i = pl.program_id(1)

        @pl.when(jnp.logical_and(j == 0, i == 0))
        def _():
            dq_ref[...] = jnp.zeros_like(dq_ref)

        @pl.when(i == 0)
        def _():
            dk_sc[...] = jnp.zeros_like(dk_sc)
            dv_sc[...] = jnp.zeros_like(dv_sc)
            dcr_sc[...] = jnp.zeros_like(dcr_sc)

        def tile(on_diagonal):
            causal = _causal_tile(t) if on_diagonal else None
            left = lax.broadcasted_iota(jnp.int32, (1, LANES), 1) < HEAD
            qrows = pl.ds(pl.multiple_of(i * t, t), t)
            for p in range(N_PAIRS):
                lanes = slice(p * LANES, (p + 1) * LANES)
                q2, k2, v2, do2 = q_ref[:, lanes], k_ref[:, lanes], v_ref[:, lanes], do_ref[:, lanes]
                zero = jnp.zeros_like(q2)
                dq2 = jnp.zeros((t, LANES), F32)
                dk2 = jnp.zeros((t, LANES), F32)
                dv2 = jnp.zeros((t, LANES), F32)
                for e in range(2):
                    h = 2 * p + e
                    msk = left if e == 0 else jnp.logical_not(left)
                    bias = _wide(cc_ref[h]) - cr_ref[h:h + 1, :]
                    qh, doh, pm, dp = _attn_probs(q2, k2, v2, do2, msk, causal, bias, lse_ref[h])
                    dsc = pm * (dp - _wide(dd_ref[h]))
                    dsb = dsc.astype(BF16)
                    dv2 += _dot_tn(pm.astype(BF16), doh)
                    dk2 += _dot_tn(dsb, qh)
                    dq2 += jnp.dot(dsb, jnp.where(msk, k2, zero), preferred_element_type=F32)
                    dcr_sc[h:h + 1, :] += -_colsum(dsc)
                dq_ref[qrows, lanes] += dq2 * ATT_SCALE
                dk_sc[:, lanes] += dk2
                dv_sc[:, lanes] += dv2

        pl.when(i > j)(functools.partial(tile, False))
        pl.when(i == j)(functools.partial(tile, True))

        @pl.when(i == nblk - 1)
        def _():
            dk_ref[...] = dk_sc[...]
            dv_ref[...] = dv_sc[...]
            dcr_ref[...] = dcr_sc[...]

    qblk = pl.BlockSpec((t, D_HALF), lambda j, i: (jnp.maximum(i, j), 0))
    qcol = pl.BlockSpec((N_HEADS, t, LANES), lambda j, i: (0, jnp.maximum(i, j), 0))
    kblk = pl.BlockSpec((t, D_HALF), lambda j, i: (j, 0))
    return pl.pallas_call(
        body, name="fox_attn_bwd", grid=(nblk, nblk),
        in_specs=[qblk, kblk, kblk, qblk, qcol, qcol, qcol, pl.BlockSpec((N_HEADS, t), lambda j, i: (0, j))],
        out_specs=[pl.BlockSpec((s, D_HALF), lambda j, i: (0, 0)), kblk, kblk,
                   pl.BlockSpec((N_HEADS, t), lambda j, i: (0, j))],
        out_shape=[jax.ShapeDtypeStruct((s, D_HALF), F32)] * 3 + [jax.ShapeDtypeStruct((N_HEADS, s), F32)],
        scratch_shapes=[pltpu.VMEM((t, D_HALF), F32), pltpu.VMEM((t, D_HALF), F32), pltpu.VMEM((N_HEADS, t), F32)],
        compiler_params=_params("arbitrary", "arbitrary"),
    )(q, k, v, do, lse, dd, cc, cr)


def _fox_prep_bwd(u_b, dq, dk, dv, dgate, dcum, fb, qg, kg, tm=256):
    s = u_b.shape[0]
    nb = s // tm

    def body(ub_ref, dq_ref, dk_ref, dv_ref, dg_ref, dc_ref, fb_ref, qg_ref, kg_ref,
             du_ref, dqg_ref, dkg_ref, dfb_ref, carry):
        i = pl.program_id(0)

        @pl.when(i == 0)
        def _():
            carry[...] = jnp.zeros_like(carry)
            dqg_ref[...] = jnp.zeros_like(dqg_ref)
            dkg_ref[...] = jnp.zeros_like(dkg_ref)
            dfb_ref[...] = jnp.zeros_like(dfb_ref)

        bd = _head_ones()
        for lo, g_ref, d_ref, dgain_ref in ((0, qg_ref, dq_ref, dqg_ref), (512, kg_ref, dk_ref, dkg_ref)):
            gain = g_ref[...]
            xh, rinv, _ = _head_rms(ub_ref[:, lo:lo + 512], gain, bd)
            dn = d_ref[...]
            dgain_ref[...] += _colsum(dn * xh)
            dxh = dn * gain
            du_ref[:, lo:lo + 512] = rinv * (dxh - xh * (_head_sum(dxh * xh, bd) * (1.0 / HEAD)))
        du_ref[:, 1024:1536] = dv_ref[...]
        du_ref[:, 1536:2048] = dg_ref[...]
        lane = lax.broadcasted_iota(jnp.int32, (1, LANES), 1)
        dc = dc_ref[...]
        dlogf = _exact_dot(dc, _tri(tm, False), ones_first=True) + carry[...]
        carry[...] += _colsum(dc)
        fl = ub_ref[:, 2048:2176] + fb_ref[...]
        dfl = jnp.where(lane < N_HEADS, dlogf * (1.0 - _sigmoid(fl)), 0.0)
        du_ref[:, 2048:2176] = dfl
        dfb_ref[...] += _colsum(dfl)

    rev = lambda w: pl.BlockSpec((tm, w), lambda i: (nb - 1 - i, 0))
    vec = lambda w: pl.BlockSpec((1, w), lambda i: (0, 0))
    return pl.pallas_call(
        body, name="fox_prep_bwd", grid=(nb,),
        in_specs=[rev(SEC)] + [rev(D_HALF)] * 4 + [rev(LANES), vec(LANES), vec(D_HALF), vec(D_HALF)],
        out_specs=[rev(SEC), vec(D_HALF), vec(D_HALF), vec(LANES)],
        out_shape=[jax.ShapeDtypeStruct((s, SEC), F32), jax.ShapeDtypeStruct((1, D_HALF), F32),
                   jax.ShapeDtypeStruct((1, D_HALF), F32), jax.ShapeDtypeStruct((1, LANES), F32)],
        scratch_shapes=[pltpu.VMEM((1, LANES), F32)],
        compiler_params=_params("arbitrary"),
    )(u_b, dq, dk, dv, dgate, dcum, fb, qg, kg)


def _merge(mix_a, mix_b, u_g, x, tgt, wa, wb, wo, fg, tm=256):
    s, d = x.shape

    def body(ma_ref, mb_ref, ug_ref, x_ref, t_ref, wa_ref, wb_ref, wo_ref, fg_ref,
             dx2_ref, dma_ref, dmb_ref, dug_ref, dwa_ref, dwb_ref, dwo_ref, dfg_ref, loss_ref):
        i = pl.program_id(0)

        @pl.when(i == 0)
        def _():
            for ref in (dwa_ref, dwb_ref, dwo_ref, dfg_ref, loss_ref):
                ref[...] = jnp.zeros_like(ref)

        wa_v, wb_v, wo_v, fg_v = wa_ref[...], wb_ref[...], wo_ref[...], fg_ref[...]
        ma = ma_ref[...].astype(BF16)
        mb = mb_ref[...].astype(BF16)
        ya = jnp.dot(ma, wa_v, preferred_element_type=F32)
        yb = jnp.dot(mb, wb_v, preferred_element_type=F32)
        sa = _sigmoid(ug_ref[:, 0:d])
        sb = _sigmoid(ug_ref[:, d:2 * d])
        merged = (sa * ya + sb * yb).astype(BF16)
        x2 = x_ref[...] + jnp.dot(merged, wo_v, preferred_element_type=F32)
        r2 = lax.rsqrt(jnp.mean(x2 * x2, axis=-1, keepdims=True) + RMS_EPS)
        x2h = x2 * r2
        err = x2h * fg_v - t_ref[...]
        loss_ref[...] += _colsum(err * err)
        dy = err * (1.0 / d)
        dfg_ref[...] += _colsum(dy * x2h)
        dx2h = dy * fg_v
        dx2 = r2 * (dx2h - x2h * jnp.mean(dx2h * x2h, axis=-1, keepdims=True))
        dx2_ref[...] = dx2
        dx2b = dx2.astype(BF16)
        dmerged = _dot_nt(dx2b, wo_v)
        dwo_ref[...] += _dot_tn(merged, dx2b)
        dya = dmerged * sa
        dyb = dmerged * sb
        dug_ref[:, 0:d] = dya * ya * (1.0 - sa)
        dug_ref[:, d:2 * d] = dyb * yb * (1.0 - sb)
        dyab = dya.astype(BF16)
        dybb = dyb.astype(BF16)
        dma_ref[...] = _dot_nt(dyab, wa_v)
        dmb_ref[...] = _dot_nt(dybb, wb_v)
        dwa_ref[...] += _dot_tn(ma, dyab)
        dwb_ref[...] += _dot_tn(mb, dybb)

    row = lambda w: pl.BlockSpec((tm, w), lambda i: (i, 0))
    full = lambda a: pl.BlockSpec(a.shape, lambda i: (0, 0))
    fshape = lambda a: jax.ShapeDtypeStruct(a.shape, F32)
    return pl.pallas_call(
        body, name="merge_fwd_bwd", grid=(s // tm,),
        in_specs=[row(D_HALF), row(D_HALF), row(GATE_COLS), row(d), row(d), full(wa), full(wb), full(wo), full(fg)],
        out_specs=[row(d), row(D_HALF), row(D_HALF), row(GATE_COLS), full(wa), full(wb), full(wo), full(fg), full(fg)],
        out_shape=[jax.ShapeDtypeStruct((s, d), F32), jax.ShapeDtypeStruct((s, D_HALF), F32),
                   jax.ShapeDtypeStruct((s, D_HALF), F32), jax.ShapeDtypeStruct((s, GATE_COLS), F32),
                   fshape(wa), fshape(wb), fshape(wo), fshape(fg), fshape(fg)],
        compiler_params=_params("arbitrary"),
    )(mix_a, mix_b, u_g, x, tgt, wa, wb, wo, fg)


def _lora_weight(w_up, a_up):
    z = jnp.zeros((LORA, D_HALF), w_up.dtype)
    return jnp.concatenate([jnp.concatenate([w_up, z], axis=1), jnp.concatenate([z, a_up], axis=1)], axis=0)


def _device_grads(x, tgt, p, w_a, w_up, a_up, late_weights, fwd_exchange=None, bwd_exchange=None, tail_exchange=None):
    wl = _lora_weight(w_up, a_up)
    rk = p["r_k"].reshape(1, D_HALF)
    fb = jnp.pad(p["f_bias"], ((0, 0), (0, LANES - N_HEADS)))
    qg = jnp.tile(p["q_norm_g"], (1, N_HEADS))
    kg = jnp.tile(p["k_norm_g"], (1, N_HEADS))
    fg = p["final_norm_g"].reshape(1, D_MODEL)
    mixer = (p["shift_mu"], wl, p["w0"], p["a0"], p["k_k"], p["k_a"])

    h = _rmsnorm_in(x, p["norm_g"])
    u_a = _matmul_nn(h, w_a, "inproj_rwkv")
    r, dec, k, v, av, bv, gate_a = _rwkv_prep(u_a, *mixer)
    y, st, arrived = _wkv_fwd(r, dec, k, av, bv, v, fwd_exchange)
    mix_a = _rwkv_post(y, r, k, v, gate_a, p["lnx_w"], p["lnx_b"], rk)

    w_b, w_g, w_out_a, w_out_b, w_out = late_weights(arrived)
    u_b = _matmul_nn(h, w_b, "inproj_fox")
    u_g = _matmul_nn(h, w_g, "inproj_gate")
    q, kn, vb, cc, cr = _fox_prep(u_b, fb, qg, kg)
    o, mix_b, lse = _attn_fwd(q, kn, vb, cc, cr, u_b)

    dx2, dmix_a, dmix_b, du_g, dwa, dwb, dwo, dfg, loss_vec = _merge(
        mix_a, mix_b, u_g, x, tgt, w_out_a, w_out_b, w_out, fg)

    do, dgate_b = _fox_post_bwd(dmix_b, o, u_b)
    dd = _attn_bwd_rowdot(q, kn, vb, do, lse, cc, cr)
    dq, dk_att, dv_att, dcr = _attn_bwd(q, kn, vb, do, lse, dd, cc, cr)
    dcum = jnp.pad(dcr.T, ((0, 0), (0, LANES - N_HEADS)))
    du_b, dqg, dkg, dfb = _fox_prep_bwd(u_b, dq, dk_att, dv_att, dgate_b, dcum, fb, qg, kg)
    h_t = h.T
    dw_b = _matmul_tn_acc(h_t, du_b, "dw_fox")
    dw_g = _matmul_tn_acc(h_t, du_g, "dw_gate")

    dy, dr_b, dk_b, dv_b, dgate_a, dlw, dlb, drk = _rwkv_post_bwd(
        dmix_a, y, r, k, v, gate_a, p["lnx_w"], p["lnx_b"], rk)
    scan_grads, sent = _wkv_bwd(r, dec, k, av, bv, v, dy, st,
                                bwd_exchange(dw_b, dw_g, dwa, dwb, dwo) if bwd_exchange else None)
    du_a, dmu, dwl, dw0, da0, dkkw, dkaw = _rwkv_prep_bwd(u_a, (*scan_grads, dr_b, dk_b, dv_b, dgate_a), *mixer)
    dw_a = _matmul_tn_acc(h_t, du_a, "dw_rwkv")
    dw_up, da_up = dwl[:LORA, :D_HALF], dwl[LORA:, D_HALF:]
    sent_last = _run_on_sequencer(tail_exchange(dw_a, dw_up, da_up), "scatter_tail", 1) if tail_exchange else []
    grad_x, dnorm_g, _ = _inproj_bwd(du_a, du_b, du_g, w_a, w_b, w_g, x, dx2, p["norm_g"])

    grads = dict(
        norm_g=dnorm_g, w_in=(dw_a, dw_b, dw_g), shift_mu=dmu,
        w_lora_up=dw_up, w0=dw0, a_lora_up=da_up, a0=da0, k_k=dkkw, k_a=dkaw,
        r_k=drk.reshape(1, N_HEADS, HEAD), lnx_w=dlw, lnx_b=dlb, f_bias=dfb[:, :N_HEADS],
        q_norm_g=dqg.reshape(N_HEADS, HEAD).sum(axis=0, keepdims=True),
        k_norm_g=dkg.reshape(N_HEADS, HEAD).sum(axis=0, keepdims=True),
        w_out_a=dwa, w_out_b=dwb, w_out=dwo, final_norm_g=dfg.reshape(D_MODEL))
    return loss_vec, grad_x, grads, sent, sent_last


CHIP_FLIPS = ((1, 0), (0, 1), (1, 1))
ANY = pl.BlockSpec(memory_space=pl.ANY)


def _position():
    return lax.axis_index("x"), lax.axis_index("y"), lax.axis_index("c")


def _flip(v, f):
    return 1 - v if f else v


def _both(a, b):
    if a is None:
        return b
    return a if b is None else jnp.logical_and(a, b)


def _when(cond, fn):
    if cond is None:
        fn()
    else:
        pl.when(cond)(fn)


class _Moves:
    def __init__(self, send_sems, recv_sems, local_sems):
        self.send_sems, self.recv_sems, self.local_sems = send_sems, recv_sems, local_sems
        self.remote, self.local = [], []

    def send(self, src, dst, peer, landing, send_if=None, recv_if=None):
        k = len(self.remote)
        sems = dict(send_sem=self.send_sems.at[k], recv_sem=self.recv_sems.at[k], device_id=peer, device_id_type=MESH)
        out = pltpu.make_async_remote_copy(src_ref=src, dst_ref=dst, **sems)
        arrival = pltpu.make_async_remote_copy(src_ref=src, dst_ref=landing, **sems)
        self.remote.append((out, arrival, send_if, recv_if))

    def copy(self, src, dst, cond=None):
        cp = pltpu.make_async_copy(src, dst, self.local_sems.at[len(self.local)])
        self.local.append((cp, cond))

    def start(self, also=None):
        for cp, cond in self.local:
            _when(_both(also, cond), cp.start)
        for out, _, send_if, _ in self.remote:
            _when(_both(also, send_if), out.start)

    def wait_arrivals(self, also=None):
        for _, arrival, _, recv_if in self.remote:
            _when(_both(also, recv_if), arrival.wait_recv)

    def wait_sent(self, also=None):
        for out, _, send_if, _ in self.remote:
            _when(_both(also, send_if), out.wait_send)
        for cp, cond in self.local:
            _when(_both(also, cond), cp.wait)

    def wait(self, also=None):
        self.wait_arrivals(also)
        self.wait_sent(also)


class _Exchange:
    def __init__(self, operands, out_shapes, n_remote, n_local, build, n_relay=0, relay=None):
        self.operands, self.out_shapes = list(operands), list(out_shapes)
        self.n_remote, self.n_local, self.build = n_remote, n_local, build
        self.n_relay, self.relay = n_relay, relay

    def scratch(self):
        return [pltpu.SemaphoreType.DMA((self.n_remote,)), pltpu.SemaphoreType.DMA((self.n_remote,)),
                pltpu.SemaphoreType.DMA((max(self.n_local, 1),))]

    def moves(self, in_refs, out_refs, sems):
        mv = _Moves(*sems)
        self.build(mv, in_refs, out_refs)
        return mv

    def run_alone(self, name):
        n_in, n_out = len(self.operands), len(self.out_shapes)
        relay_scratch = [pltpu.SemaphoreType.DMA((self.n_relay,))] * 2 if self.relay else []

        def body(*refs):
            ins, outs, sems = refs[:n_in], refs[n_in:n_in + n_out], refs[n_in + n_out:]
            mv = self.moves(ins, outs, sems[:3])
            mv.start()
            mv.wait_arrivals()
            if self.relay:
                passed = _Moves(sems[3], sems[4], None)
                self.relay(passed, ins, outs)
                passed.start()
                passed.wait()
            mv.wait_sent()

        return pl.pallas_call(
            body, name=name, in_specs=[ANY] * n_in, out_specs=[ANY] * n_out, out_shape=self.out_shapes,
            scratch_shapes=self.scratch() + relay_scratch, compiler_params=pltpu.CompilerParams(has_side_effects=True),
        )(*self.operands)


def _run_on_sequencer(exchange, name, collective_id):
    ins = [jax.new_ref(a, memory_space=pltpu.MemorySpace.HBM) for a in exchange.operands]
    outs = [jax.empty_ref(s, memory_space=pltpu.MemorySpace.HBM) for s in exchange.out_shapes]
    relay_scratch = [pltpu.SemaphoreType.DMA((exchange.n_relay,))] * 2 if exchange.relay else []

    def launch(*sems):
        x, y, c = _position()
        peers = [(_flip(x, fx), _flip(y, fy), c) for fx, fy in CHIP_FLIPS] + ([(x, y, 1 - c)] if exchange.relay else [])
        barrier = pltpu.get_barrier_semaphore()
        for peer in peers:
            pl.semaphore_signal(barrier, inc=1, device_id=peer, device_id_type=MESH)
        pl.semaphore_wait(barrier, len(peers))
        moves = exchange.moves(ins, outs, sems[:3])
        moves.start()
        moves.wait_arrivals()
        if exchange.relay:
            passed = _Moves(sems[3], sems[4], None)
            exchange.relay(passed, ins, outs)
            passed.start()
            passed.wait()
        moves.wait_sent()

    pl.kernel(launch, mesh=plsc.ScalarSubcoreMesh(axis_name="sequencer", num_cores=1), name=name,
              scratch_types=tuple(exchange.scratch() + relay_scratch),
              compiler_params=pltpu.CompilerParams(collective_id=collective_id))()
    return [o[...] for o in outs]


def _row_major_copy(a, name):
    def body(a_ref, o_ref, sem):
        cp = pltpu.make_async_copy(a_ref, o_ref, sem)
        cp.start()
        cp.wait()

    return pl.pallas_call(body, name=name, in_specs=[ANY], out_specs=ANY, out_shape=jax.ShapeDtypeStruct(a.shape, a.dtype),
                          scratch_shapes=[pltpu.SemaphoreType.DMA])(a)


def _is_chip(x, y, chip):
    return jnp.logical_and(x == chip // 2, y == chip % 2)


def _gather_exchange(from_chip, from_all, split=()):
    n1, n2 = len(from_chip), len(from_all)

    def rows_of(t, c):
        half = from_chip[t][1].shape[0] // 2
        return pl.ds(c * half, half)

    def build(mv, ins, outs):
        x, y, c = _position()
        me = 2 * x + y
        for t, (chip, _) in enumerate(from_chip):
            mv.copy(ins[t], outs[t], cond=_is_chip(x, y, chip))
        for t in range(n2):
            mv.copy(ins[n1 + t], outs[n1 + t].at[me])
        for fx, fy in CHIP_FLIPS:
            px, py = _flip(x, fx), _flip(y, fy)
            peer = (px, py, c)
            for t, (chip, _) in enumerate(from_chip):
                part = rows_of(t, c) if t in split else slice(None)
                mv.send(ins[t].at[part], outs[t].at[part], peer, landing=outs[t].at[part],
                        send_if=_is_chip(x, y, chip), recv_if=_is_chip(px, py, chip))
            for t in range(n2):
                mv.send(ins[n1 + t], outs[n1 + t].at[me], peer, landing=outs[n1 + t].at[2 * px + py])

    def relay(mv, ins, outs):
        x, y, c = _position()
        for t in split:
            came = jnp.logical_not(_is_chip(x, y, from_chip[t][0]))
            mv.send(outs[t].at[rows_of(t, c)], outs[t].at[rows_of(t, c)], (x, y, 1 - c),
                    landing=outs[t].at[rows_of(t, 1 - c)], send_if=came, recv_if=came)

    arrays = [a for _, a in from_chip] + list(from_all)
    shapes = [jax.ShapeDtypeStruct(a.shape, a.dtype) for _, a in from_chip]
    shapes += [jax.ShapeDtypeStruct((N_CHIPS,) + a.shape, a.dtype) for a in from_all]
    return _Exchange(arrays, shapes, len(CHIP_FLIPS) * (n1 + n2), n1 + n2, build,
                     n_relay=len(split), relay=relay if split else None)


def _scatter_exchange(to_chip, to_all):
    n1, n2 = len(to_chip), len(to_all)

    def build(mv, ins, outs):
        x, y, c = _position()
        for f, (fx, fy) in enumerate(CHIP_FLIPS):
            px, py = _flip(x, fx), _flip(y, fy)
            peer = (px, py, c)
            for t, (chip, _) in enumerate(to_chip):
                mv.send(ins[t], outs[t].at[f], peer, landing=outs[t].at[f],
                        send_if=_is_chip(px, py, chip), recv_if=_is_chip(x, y, chip))
            for t in range(n2):
                mv.send(ins[n1 + t].at[2 * px + py], outs[n1 + t].at[f], peer, landing=outs[n1 + t].at[f])

    arrays = [a for _, a in to_chip] + list(to_all)
    shapes = [jax.ShapeDtypeStruct((len(CHIP_FLIPS),) + a.shape, a.dtype) for _, a in to_chip]
    shapes += [jax.ShapeDtypeStruct((len(CHIP_FLIPS),) + a.shape[1:], a.dtype) for a in to_all]
    return _Exchange(arrays, shapes, len(CHIP_FLIPS) * (n1 + n2), 0, build)


def _swap_sibling(tensors, name):
    n = len(tensors)

    def body(*refs):
        ins, outs = refs[:n], refs[n:2 * n]
        send_sems, recv_sems = refs[2 * n:]
        x, y, c = _position()
        copies = [pltpu.make_async_remote_copy(
            src_ref=ins[t], dst_ref=outs[t], send_sem=send_sems.at[t], recv_sem=recv_sems.at[t],
            device_id=(x, y, 1 - c), device_id_type=MESH) for t in range(n)]
        for cp in copies:
            cp.start()
        for cp in copies:
            cp.wait_recv()
        for cp in copies:
            cp.wait_send()

    return pl.pallas_call(
        body, name=name, in_specs=[ANY] * n, out_specs=[ANY] * n,
        out_shape=[jax.ShapeDtypeStruct(a.shape, a.dtype) for a in tensors],
        scratch_shapes=[pltpu.SemaphoreType.DMA((n,)), pltpu.SemaphoreType.DMA((n,))],
        compiler_params=pltpu.CompilerParams(has_side_effects=True),
    )(*tensors)


def _allreduce_small(slab):
    stages = 3

    def body(x_ref, o_ref, buf, send_sems, recv_sems):
        x, y, c = _position()
        peers = ((1 - x, y, c), (x, 1 - y, c), (x, y, 1 - c))
        o_ref[...] = x_ref[...]
        for k, peer in enumerate(peers):
            cp = pltpu.make_async_remote_copy(src_ref=o_ref, dst_ref=buf.at[k], send_sem=send_sems.at[k],
                                              recv_sem=recv_sems.at[k], device_id=peer, device_id_type=MESH)
            cp.start()
            cp.wait()
            o_ref[...] = o_ref[...] + buf[k]

    return pl.pallas_call(
        body, name="allreduce_small",
        in_specs=[pl.BlockSpec(memory_space=pltpu.VMEM)], out_specs=pl.BlockSpec(memory_space=pltpu.VMEM),
        out_shape=jax.ShapeDtypeStruct(slab.shape, slab.dtype),
        scratch_shapes=[pltpu.VMEM((stages,) + slab.shape, slab.dtype),
                        pltpu.SemaphoreType.DMA((stages,)), pltpu.SemaphoreType.DMA((stages,))],
        compiler_params=pltpu.CompilerParams(has_side_effects=True),
    )(slab)


def _row_tile(r):
    return min(r, 256)


def _sum4(stack, recv, me):
    _, r, c = stack.shape
    tr = _row_tile(r)

    def body(me_ref, own_ref, recv_ref, o_ref):
        o_ref[...] = (((own_ref[...] + recv_ref[0].astype(F32)) + recv_ref[1].astype(F32))
                      + recv_ref[2].astype(F32))

    return pl.pallas_call(
        body, name="sum_partials",
        grid_spec=pltpu.PrefetchScalarGridSpec(
            num_scalar_prefetch=1, grid=(r // tr,),
            in_specs=[pl.BlockSpec((None, tr, c), lambda i, me_ref: (me_ref[0], i, 0)),
                      pl.BlockSpec((len(CHIP_FLIPS), tr, c), lambda i, me_ref: (0, i, 0))],
            out_specs=pl.BlockSpec((tr, c), lambda i, me_ref: (i, 0))),
        out_shape=jax.ShapeDtypeStruct((r, c), F32), compiler_params=_params("parallel"),
    )(me, stack, recv)


def _sum_block(own, recv):
    r, c = own.shape
    tr = _row_tile(r)

    def body(own_ref, recv_ref, o_ref):
        o_ref[...] = (((own_ref[...] + recv_ref[0].astype(F32)) + recv_ref[1].astype(F32))
                      + recv_ref[2].astype(F32))

    return pl.pallas_call(
        body, name="sum_block", grid=(r // tr,),
        in_specs=[pl.BlockSpec((tr, c), lambda i: (i, 0)), pl.BlockSpec((len(CHIP_FLIPS), tr, c), lambda i: (0, i, 0))],
        out_specs=pl.BlockSpec((tr, c), lambda i: (i, 0)),
        out_shape=jax.ShapeDtypeStruct((r, c), F32), compiler_params=_params("parallel"),
    )(own, recv)


def _adamw_math(w, g, m, v):
    m = ADAM_B1 * m + (1.0 - ADAM_B1) * g
    v = ADAM_B2 * v + (1.0 - ADAM_B2) * (g * g)
    m_hat = m / (1.0 - ADAM_B1 ** ADAM_STEP)
    v_hat = v / (1.0 - ADAM_B2 ** ADAM_STEP)
    delta = -ADAM_LR * (m_hat / (jnp.sqrt(v_hat) + ADAM_EPS) + ADAM_WD * w)
    return delta, m, v


def _adamw(w, m, v, g_parts, name):
    r, c = w.shape
    tr = _row_tile(r)
    n = len(g_parts)

    def body(*refs):
        w_ref, m_ref, v_ref = refs[:3]
        g_refs = refs[3:3 + n]
        g_out, d_out, m_out, v_out = refs[3 + n:]
        g = g_refs[0][...]
        for ref in g_refs[1:]:
            g = g + ref[...]
        g_out[...] = g
        d_out[...], m_out[...], v_out[...] = _adamw_math(w_ref[...], g, m_ref[...], v_ref[...])

    blk = pl.BlockSpec((tr, c), lambda i: (i, 0))
    return pl.pallas_call(
        body, name=name, grid=(r // tr,), in_specs=[blk] * (3 + n), out_specs=[blk] * 4,
        out_shape=[jax.ShapeDtypeStruct((r, c), F32)] * 4, compiler_params=_params("parallel"),
    )(w, m, v, *g_parts)


def _adamw_small(total, w, m, v):
    sizes = [w[n].size for n in SMALL]
    flat = lambda d: [d[n].reshape(1, -1) for n in SMALL]
    k = len(SMALL)

    def body(*refs):
        total_ref, w_refs, m_refs, v_refs = refs[0], refs[1:1 + k], refs[1 + k:1 + 2 * k], refs[1 + 2 * k:1 + 3 * k]
        outs = refs[1 + 3 * k:]
        for i, size in enumerate(sizes):
            g = total_ref[i:i + 1, 0:size]
            outs[i][...] = g
            outs[k + i][...], outs[2 * k + i][...], outs[3 * k + i][...] = _adamw_math(
                w_refs[i][...], g, m_refs[i][...], v_refs[i][...])

    res = pl.pallas_call(
        body, name="adamw_small", out_shape=[jax.ShapeDtypeStruct((1, size), F32) for size in sizes] * 4,
        compiler_params=_params(),
    )(total, *flat(w), *flat(m), *flat(v))
    return [{n: res[j * k + i].reshape(w[n].shape) for i, n in enumerate(SMALL)} for j in range(4)]


SHARDED = ("w_in", "w_lora_up", "a_lora_up", "w_out_a", "w_out_b", "w_out")
ROW_SHARDED = ("w_out",)
SMALL = ("norm_g", "shift_mu", "w0", "a0", "k_k", "k_a", "r_k", "lnx_w", "lnx_b", "f_bias", "q_norm_g", "k_norm_g",
         "final_norm_g")
WEIGHTS = ("norm_g", "w_in", "shift_mu", "w_lora_up", "w0", "a_lora_up", "a0", "k_k", "k_a", "r_k", "lnx_w", "lnx_b",
           "f_bias", "q_norm_g", "k_norm_g", "w_out_a", "w_out_b", "w_out", "final_norm_g")
SLAB_ROWS = 16
SLAB_COLS = SEC


def _to_slab(named, extra=None):
    rows = [jnp.pad(named[n].reshape(1, -1), ((0, 0), (0, SLAB_COLS - named[n].size))) for n in SMALL]
    if extra is not None:
        rows.append(jnp.pad(extra.reshape(1, -1), ((0, 0), (0, SLAB_COLS - extra.size))))
    rows.append(jnp.zeros((SLAB_ROWS - len(rows), SLAB_COLS), F32))
    return jnp.concatenate(rows, axis=0)


def _by_chip(g, name):
    if name in ROW_SHARDED:
        return g.reshape(N_CHIPS, g.shape[0] // N_CHIPS, g.shape[1])
    r, c = g.shape
    return g.reshape(r, N_CHIPS, c // N_CHIPS).transpose(1, 0, 2)


def _from_chips(stack, name):
    if name in ROW_SHARDED:
        return stack.reshape(-1, stack.shape[2])
    _, r, c = stack.shape
    return stack.transpose(1, 0, 2).reshape(r, N_CHIPS * c)


def kernel(x, norm_g, w_in, shift_mu, w_lora_up, w0, a_lora_up, a0, k_k, k_a, r_k, lnx_w, lnx_b, f_bias, q_norm_g, k_norm_g, w_out_a, w_out_b, w_out, final_norm_g, loss_target, m_norm_g, m_w_in, m_shift_mu, m_w_lora_up, m_w0, m_a_lora_up, m_a0, m_k_k, m_k_a, m_r_k, m_lnx_w, m_lnx_b, m_f_bias, m_q_norm_g, m_k_norm_g, m_w_out_a, m_w_out_b, m_w_out, m_final_norm_g, v_norm_g, v_w_in, v_shift_mu, v_w_lora_up, v_w0, v_a_lora_up, v_a0, v_k_k, v_k_a, v_r_k, v_lnx_w, v_lnx_b, v_f_bias, v_q_norm_g, v_k_norm_g, v_w_out_a, v_w_out_b, v_w_out, v_final_norm_g):
    w = dict(norm_g=norm_g, w_in=w_in, shift_mu=shift_mu, w_lora_up=w_lora_up, w0=w0, a_lora_up=a_lora_up, a0=a0,
             k_k=k_k, k_a=k_a, r_k=r_k, lnx_w=lnx_w, lnx_b=lnx_b, f_bias=f_bias, q_norm_g=q_norm_g,
             k_norm_g=k_norm_g, w_out_a=w_out_a, w_out_b=w_out_b, w_out=w_out, final_norm_g=final_norm_g)
    m = dict(norm_g=m_norm_g, w_in=m_w_in, shift_mu=m_shift_mu, w_lora_up=m_w_lora_up, w0=m_w0,
             a_lora_up=m_a_lora_up, a0=m_a0, k_k=m_k_k, k_a=m_k_a, r_k=m_r_k, lnx_w=m_lnx_w, lnx_b=m_lnx_b,
             f_bias=m_f_bias, q_norm_g=m_q_norm_g, k_norm_g=m_k_norm_g, w_out_a=m_w_out_a, w_out_b=m_w_out_b,
             w_out=m_w_out, final_norm_g=m_final_norm_g)
    v = dict(norm_g=v_norm_g, w_in=v_w_in, shift_mu=v_shift_mu, w_lora_up=v_w_lora_up, w0=v_w0,
             a_lora_up=v_a_lora_up, a0=v_a0, k_k=v_k_k, k_a=v_k_a, r_k=v_r_k, lnx_w=v_lnx_w, lnx_b=v_lnx_b,
             f_bias=v_f_bias, q_norm_g=v_q_norm_g, k_norm_g=v_k_norm_g, w_out_a=v_w_out_a, w_out_b=v_w_out_b,
             w_out=v_w_out, final_norm_g=v_final_norm_g)
    shapes = {n: w[n].shape for n in WEIGHTS}

    shard = {n: w[n][0].astype(BF16) for n in SHARDED}
    late = ("w_out_a", "w_out_b", "w_out")
    loras = ("w_lora_up", "a_lora_up")
    w_in_head, w_in_tail = shard["w_in"][:, :A_TAIL], shard["w_in"][:, A_TAIL:]
    shard0, shard1_head, up_stack, aup_stack = _run_on_sequencer(_gather_exchange(
        [(0, shard["w_in"]), (1, w_in_head)], [shard[n] for n in loras], split=(0,)), "gather_early", 2)
    moments = (_row_major_copy(m["w_in"][0], "m_w_in_rows"), _row_major_copy(v["w_in"][0], "v_w_in_rows"))
    shard0, moments = lax.optimization_barrier((shard0, moments))
    w_a = jnp.concatenate([shard0, shard1_head], axis=1)

    def late_weights(arrived):
        shard1_tail, shard2, shard3 = arrived[:3]
        w_b = jnp.concatenate([shard1_tail, shard2[:, :B_TAIL], jnp.zeros((D_MODEL, SEC - FOX_REAL), BF16)], axis=1)
        w_g = jnp.concatenate([shard2[:, B_TAIL:], shard3], axis=1)
        return (w_b, w_g, *[_from_chips(s, n) for n, s in zip(late, arrived[3:])])

    own = {}

    def bwd_exchange(dw_b, dw_g, dwa, dwb, dwo):
        own["tail1"] = dw_b[:, :B_HEAD]
        own["block2"] = jnp.concatenate([dw_b[:, B_HEAD:FOX_REAL], dw_g[:, :G_HEAD]], axis=1)
        own["block3"] = dw_g[:, G_HEAD:]
        own.update({n: _by_chip(g, n) for n, g in zip(late, (dwa, dwb, dwo))})
        return _scatter_exchange([(1, own["tail1"].astype(BF16)), (2, own["block2"].astype(BF16)),
                                  (3, own["block3"].astype(BF16))], [own[n].astype(BF16) for n in late])

    def tail_exchange(dw_a, dw_up, da_up):
        own["block0"], own["head1"] = dw_a[:, :SHARD_COLS], dw_a[:, SHARD_COLS:]
        own.update({n: _by_chip(g, n) for n, g in zip(loras, (dw_up, da_up))})
        return _scatter_exchange([(0, own["block0"].astype(BF16)), (1, own["head1"].astype(BF16))],
                                 [own[n].astype(BF16) for n in loras])

    small = {n: w[n] for n in SMALL}
    loss_vec, grad_x, grads, sent, sent_last = _device_grads(
        x[0], loss_target[0], small, w_a, _from_chips(up_stack, "w_lora_up"), _from_chips(aup_stack, "a_lora_up"),
        late_weights, _gather_exchange([(1, w_in_tail), (2, shard["w_in"]), (3, shard["w_in"])], [shard[n] for n in late]),
        bwd_exchange, tail_exchange)

    total = _allreduce_small(_to_slab(grads, extra=loss_vec))
    loss = (0.5 / D_MODEL) * jnp.sum(total[len(SMALL)])
    out_g, out_d, out_m, out_v = _adamw_small(total, w, m, v)

    xpos, ypos, _ = _position()
    me = (2 * xpos + ypos).astype(jnp.int32).reshape(1)
    core_sum = {n: _sum4(own[n], r, me) for n, r in zip(late, sent[3:])}
    theirs = dict(zip(late, _swap_sibling([core_sum[n] for n in late], "swap_sibling_early")))
    sent_last, out_d["norm_g"], theirs = lax.optimization_barrier((sent_last, out_d["norm_g"], theirs))
    core_sum["w_in"] = lax.switch(me[0], [
        lambda: _sum_block(own["block0"], sent_last[0]),
        lambda: jnp.concatenate([_sum_block(own["head1"], sent_last[1]), _sum_block(own["tail1"], sent[0])], axis=1),
        lambda: _sum_block(own["block2"], sent[1]),
        lambda: _sum_block(own["block3"], sent[2])])
    core_sum.update({n: _sum4(own[n], r, me) for n, r in zip(loras, sent_last[2:])})
    rest = ("w_in",) + loras
    theirs.update(zip(rest, _swap_sibling([core_sum[n] for n in rest], "swap_sibling")))
    for n in SHARDED:
        m_n, v_n = moments if n == "w_in" else (m[n][0], v[n][0])
        g, d, m2, v2 = _adamw(w[n][0], m_n, v_n, [core_sum[n], theirs[n]], "adamw_" + n)
        out_g[n], out_d[n], out_m[n], out_v[n] = (a.reshape(shapes[n]) for a in (g, d, m2, v2))

    return (loss, grad_x.reshape(x.shape), *[out_g[n] for n in WEIGHTS], *[out_d[n] for n in WEIGHTS],
            *[out_m[n] for n in WEIGHTS], *[out_v[n] for n in WEIGHTS])
```

```python
import functools
import math

import jax
import jax.numpy as jnp
from jax import lax
from jax.experimental import pallas as pl
from jax.experimental.pallas import tpu as pltpu
from jax.experimental.pallas import tpu_sc as plsc

F32 = jnp.float32
BF16 = jnp.bfloat16

D_MODEL = 1024
D_HALF = 512
HEAD = 64
N_HEADS = 8
LORA = 64
RWKV_COLS = 2176
FOX_REAL = 2056
SEC = 2176
GATE_COLS = 2048
IN_COLS = 6280
N_CHIPS = 4
SHARD_COLS = IN_COLS // N_CHIPS
A_TAIL = RWKV_COLS - SHARD_COLS
B_HEAD = SHARD_COLS - A_TAIL
B_TAIL = FOX_REAL - B_HEAD
G_HEAD = SHARD_COLS - B_TAIL
RMS_EPS = 1e-6
LNX_EPS = 64e-5
ATT_SCALE = HEAD ** -0.5
NEG = -1e30

ADAM_LR = 0.001
ADAM_B1 = 0.9
ADAM_B2 = 0.999
ADAM_EPS = 1e-08
ADAM_WD = 0.01
ADAM_STEP = 10

LANES = 128
SUBLANES = 8
VMEM_LIMIT = 56 * 1024 * 1024
MESH = pl.DeviceIdType.MESH


def _params(*sem):
    return pltpu.CompilerParams(dimension_semantics=sem if sem else None, vmem_limit_bytes=VMEM_LIMIT)


def _sigmoid(x):
    return 1.0 / (1.0 + jnp.exp(-x))


def _log_sigmoid(x):
    return jnp.minimum(x, 0.0) - jnp.log(1.0 + jnp.exp(-jnp.abs(x)))


def _head_ones():
    r = lax.broadcasted_iota(jnp.int32, (LANES, LANES), 0) >> 6
    c = lax.broadcasted_iota(jnp.int32, (LANES, LANES), 1) >> 6
    return (r == c).astype(BF16)


def _split3(x):
    hi = x.astype(BF16)
    r1 = x - hi.astype(F32)
    mid = r1.astype(BF16)
    lo = (r1 - mid.astype(F32)).astype(BF16)
    return hi, mid, lo


def _exact_dot(x, ones_bf16, ones_first=False):
    out = None
    for piece in _split3(x):
        if ones_first:
            t = jnp.dot(ones_bf16, piece, preferred_element_type=F32)
        else:
            t = jnp.dot(piece, ones_bf16, preferred_element_type=F32)
        out = t if out is None else out + t
    return out


def _head_sum(x, bd):
    n = x.shape[1] // LANES
    parts = [_exact_dot(x[:, i * LANES:(i + 1) * LANES], bd) for i in range(n)]
    return parts[0] if n == 1 else jnp.concatenate(parts, axis=1)


def _dot_nt(a, b):
    return lax.dot_general(a, b, (((1,), (1,)), ((), ())), preferred_element_type=F32)


def _dot_tn(a, b):
    return lax.dot_general(a, b, (((0,), (0,)), ((), ())), preferred_element_type=F32)


def _colsum(x):
    return jnp.sum(x, axis=0, keepdims=True)


def _rmsnorm_in(x, g, tm=512):
    s, d = x.shape

    def body(x_ref, g_ref, h_ref):
        xv = x_ref[...]
        r = lax.rsqrt(jnp.mean(xv * xv, axis=-1, keepdims=True) + RMS_EPS)
        h_ref[...] = (xv * r * g_ref[...]).astype(BF16)

    return pl.pallas_call(
        body, name="rmsnorm_in", grid=(s // tm,),
        in_specs=[pl.BlockSpec((tm, d), lambda i: (i, 0)), pl.BlockSpec((1, d), lambda i: (0, 0))],
        out_specs=pl.BlockSpec((tm, d), lambda i: (i, 0)),
        out_shape=jax.ShapeDtypeStruct((s, d), BF16), compiler_params=_params("parallel"),
    )(x, g)


def _matmul_nn(a, b, name, tm=512):
    m, k = a.shape
    n = b.shape[1]

    def body(a_ref, b_ref, o_ref):
        o_ref[...] = jnp.dot(a_ref[...], b_ref[...], preferred_element_type=F32)

    return pl.pallas_call(
        body, name=name, grid=(m // tm,),
        in_specs=[pl.BlockSpec((tm, k), lambda i: (i, 0)), pl.BlockSpec((k, n), lambda i: (0, 0))],
        out_specs=pl.BlockSpec((tm, n), lambda i: (i, 0)),
        out_shape=jax.ShapeDtypeStruct((m, n), F32), compiler_params=_params("parallel"),
    )(a, b)


def _matmul_tn_acc(at, b, name, tk=512):
    m, k = at.shape
    n = b.shape[1]

    def body(a_ref, b_ref, o_ref):
        j = pl.program_id(0)

        @pl.when(j == 0)
        def _():
            o_ref[...] = jnp.zeros_like(o_ref)

        o_ref[...] += jnp.dot(a_ref[...], b_ref[...].astype(BF16), preferred_element_type=F32)

    return pl.pallas_call(
        body, name=name, grid=(k // tk,),
        in_specs=[pl.BlockSpec((m, tk), lambda j: (0, j)), pl.BlockSpec((tk, n), lambda j: (j, 0))],
        out_specs=pl.BlockSpec((m, n), lambda j: (0, 0)),
        out_shape=jax.ShapeDtypeStruct((m, n), F32), compiler_params=_params("arbitrary"),
    )(at, b)


def _inproj_bwd(du_a, du_b, du_g, w_a, w_b, w_g, x, dx2, g, exchange=None, tm=256):
    s, d = x.shape
    nb = s // tm

    def body(*refs):
        ((da_ref, db_ref, dg_ref, wa_ref, wb_ref, wg_ref, x_ref, dx2_ref, g_ref), (gx_ref, gg_ref), _,
         moves) = _split_refs(refs, 9, 2, exchange)
        i = pl.program_id(0)
        if moves:
            moves.start(also=(i == 0))

        @pl.when(i == 0)
        def _():
            gg_ref[...] = jnp.zeros_like(gg_ref)

        dh = _dot_nt(da_ref[...].astype(BF16), wa_ref[...])
        dh += _dot_nt(db_ref[...].astype(BF16), wb_ref[...])
        dh += _dot_nt(dg_ref[...].astype(BF16), wg_ref[...])
        xv = x_ref[...]
        r = lax.rsqrt(jnp.mean(xv * xv, axis=-1, keepdims=True) + RMS_EPS)
        xh = xv * r
        gg_ref[...] += _colsum(dh * xh)
        dxh = dh * g_ref[...]
        gx_ref[...] = dx2_ref[...] + r * (dxh - xh * jnp.mean(dxh * xh, axis=-1, keepdims=True))
        if moves:
            moves.wait(also=(i == nb - 1))

    row = lambda w: pl.BlockSpec((tm, w), lambda i: (i, 0))
    full = lambda a: pl.BlockSpec(a.shape, lambda i: (0, 0))
    ex_in = exchange.operands if exchange else []
    ex_out = exchange.out_shapes if exchange else []
    res = pl.pallas_call(
        body, name="inproj_bwd", grid=(nb,),
        in_specs=[row(SEC), row(SEC), row(GATE_COLS), full(w_a), full(w_b), full(w_g), row(d), row(d), full(g)]
                 + [ANY] * len(ex_in),
        out_specs=[row(d), pl.BlockSpec((1, d), lambda i: (0, 0))] + [ANY] * len(ex_out),
        out_shape=[jax.ShapeDtypeStruct((s, d), F32), jax.ShapeDtypeStruct((1, d), F32)] + ex_out,
        scratch_shapes=exchange.scratch() if exchange else [],
        compiler_params=_params("arbitrary"),
    )(du_a, du_b, du_g, w_a, w_b, w_g, x, dx2, g, *ex_in)
    return res[0], res[1], list(res[2:])


def _rwkv_elementwise(ua, prev_row, first, mu, wl, w0, a0, kkw, kaw, bd):
    tm = ua.shape[0]
    rows = lax.broadcasted_iota(jnp.int32, (tm, 1), 0)
    prev = jnp.where(first, jnp.zeros_like(prev_row), prev_row)
    shifted = jnp.where(rows == 0, prev, pltpu.roll(ua, 1, 0))
    delta = shifted - ua
    us = ua + delta * mu
    r = us[:, 0:512]
    k0 = us[:, 512:1024]
    v = us[:, 1024:1536]
    lo = us[:, 1536:1664]
    gate = us[:, 1664:2176]
    lane = lax.broadcasted_iota(jnp.int32, (1, LANES), 1)
    th = jnp.tanh(lo)
    lin = jnp.where(lane < LORA, th, lo)
    ll = jnp.dot(lin.astype(BF16), wl, preferred_element_type=F32)
    sz = _sigmoid(w0 + ll[:, :512])
    e = sz * math.exp(-0.5)
    dec = jnp.exp(-e)
    a = _sigmoid(a0 + ll[:, 512:])
    kk0 = k0 * kkw
    ss = _head_sum(kk0 * kk0, bd)
    nrm = jnp.maximum(jnp.sqrt(ss), 1e-12)
    kk = kk0 / nrm
    k = k0 * (1.0 + (a - 1.0) * kaw)
    return dict(delta=delta, us=us, r=r, k0=k0, v=v, lo=lo, gate=gate, th=th, lin=lin, sz=sz, e=e, dec=dec,
                a=a, kk0=kk0, ss=ss, nrm=nrm, kk=kk, k=k)


def _rwkv_prep(u_a, mu, wl, w0, a0, kkw, kaw, tm=256):
    s = u_a.shape[0]

    def body(ua_ref, prev_ref, mu_ref, wl_ref, w0_ref, a0_ref, kkw_ref, kaw_ref,
             r_ref, w_ref, k_ref, v_ref, a_ref, b_ref, g_ref):
        i = pl.program_id(0)
        f = _rwkv_elementwise(ua_ref[...], prev_ref[7:8, :], i == 0, mu_ref[...], wl_ref[...], w0_ref[...],
                              a0_ref[...], kkw_ref[...], kaw_ref[...], _head_ones())
        r_ref[...] = f["r"]
        w_ref[...] = f["dec"]
        k_ref[...] = f["k"]
        v_ref[...] = f["v"]
        a_ref[...] = -f["kk"]
        b_ref[...] = f["kk"] * f["a"]
        g_ref[...] = f["gate"]

    vec = lambda w: pl.BlockSpec((1, w), lambda i: (0, 0))
    out = pl.BlockSpec((tm, D_HALF), lambda i: (i, 0))
    return pl.pallas_call(
        body, name="rwkv_prep", grid=(s // tm,),
        in_specs=[pl.BlockSpec((tm, SEC), lambda i: (i, 0)),
                  pl.BlockSpec((8, SEC), lambda i: (jnp.maximum(i * (tm // 8) - 1, 0), 0)),
                  vec(SEC), pl.BlockSpec((LANES, 2 * D_HALF), lambda i: (0, 0)),
                  vec(D_HALF), vec(D_HALF), vec(D_HALF), vec(D_HALF)],
        out_specs=[out] * 7,
        out_shape=[jax.ShapeDtypeStruct((s, D_HALF), F32)] * 7,
        compiler_params=_params("parallel"),
    )(u_a, u_a, mu, wl, w0, a0, kkw, kaw)


SCAN_TB = 128
N_PAIRS = 4


def _pair_sum(x, left):
    s_l = jnp.sum(jnp.where(left, x, 0.0), axis=1, keepdims=True)
    s_r = jnp.sum(jnp.where(left, 0.0, x), axis=1, keepdims=True)
    return jnp.where(left, s_l, s_r)


def _pair_dot(x, row_l, row_r, left):
    s_l = jnp.sum(x * row_l, axis=1, keepdims=True)
    s_r = jnp.sum(x * row_r, axis=1, keepdims=True)
    return jnp.where(left, s_l, s_r)


def _halves(rows8):
    lane = lax.broadcasted_iota(jnp.int32, rows8.shape, 1)
    keep_left = (lane & (LANES - 1)) < HEAD
    return jnp.where(keep_left, rows8, 0.0), jnp.where(keep_left, 0.0, rows8)


def _quad_consts():
    lane = lax.broadcasted_iota(jnp.int32, (HEAD, 2 * LANES), 1)
    rowi = lax.broadcasted_iota(jnp.int32, (HEAD, 2 * LANES), 0)
    diag2 = rowi == (lane & (HEAD - 1))
    r = lax.broadcasted_iota(jnp.int32, (2 * LANES, 2 * LANES), 0) >> 6
    c = lax.broadcasted_iota(jnp.int32, (2 * LANES, 2 * LANES), 1) >> 6
    return diag2, (r == c).astype(BF16)


def _rows_to_columns(x8, diag2, bd2):
    lhs = jnp.concatenate([jnp.where(diag2, x8[i:i + 1], 0.0).astype(BF16) for i in range(SUBLANES)], axis=0)
    return jnp.dot(lhs, bd2, preferred_element_type=F32)


def _diag_rows(qtile, diag2, bd2, sub_row2):
    res = jnp.dot(qtile, bd2, preferred_element_type=F32)
    out = jnp.zeros((SUBLANES, 2 * LANES), F32)
    for i in range(SUBLANES):
        out = jnp.where(sub_row2 == i, _colsum(jnp.where(diag2, res[i * HEAD:(i + 1) * HEAD], 0.0)), out)
    return out


def _store_tile(qbuf, slot, p, i, x):
    qbuf[slot, p // 2, i * HEAD:(i + 1) * HEAD, (p % 2) * LANES:(p % 2 + 1) * LANES] = x.astype(BF16)


def _left_half():
    return lax.broadcasted_iota(jnp.int32, (HEAD, LANES), 1) < HEAD


def _split_refs(refs, n_rows, n_out, exchange):
    n_in = len(exchange.operands) if exchange else 0
    n_ex_out = len(exchange.out_shapes) if exchange else 0
    refs = list(refs)
    rows, refs = refs[:n_rows], refs[n_rows:]
    ex_in, refs = refs[:n_in], refs[n_in:]
    outs, refs = refs[:n_out], refs[n_out:]
    ex_out, refs = refs[:n_ex_out], refs[n_ex_out:]
    scratch, sems = (refs[:-3], refs[-3:]) if exchange else (refs, None)
    moves = exchange.moves(ex_in, ex_out, sems) if exchange else None
    return rows, outs, scratch, moves


def _wkv_fwd(r, w, k, a, b, v, exchange=None):
    s = r.shape[0]
    tb = SCAN_TB
    nb = s // tb

    def body(*refs):
        (r_ref, w_ref, k_ref, a_ref, b_ref, v_ref), (y_ref, st_ref), (state, vbuf, qbuf), moves = _split_refs(
            refs, 6, 2, exchange)
        g = pl.program_id(0)
        if moves:
            moves.start(also=(g == 0))

        @pl.when(g == 0)
        def _():
            state[...] = jnp.zeros_like(state)
            qbuf[...] = jnp.zeros_like(qbuf)

        left = _left_half()
        diag2, bd2 = _quad_consts()
        sub_row2 = lax.broadcasted_iota(jnp.int32, (SUBLANES, 2 * LANES), 0)
        groups = tb // SUBLANES
        quads = [slice(g2 * 2 * LANES, (g2 + 1) * 2 * LANES) for g2 in range(2)]

        def rows_of(q):
            return pl.ds(pl.multiple_of(q * SUBLANES, SUBLANES), SUBLANES)

        def v_tiles(q, slot):
            v8 = v_ref[rows_of(q), :]
            for g2 in range(2):
                vbuf[slot, g2] = _rows_to_columns(v8[:, quads[g2]], diag2, bd2)

        def chain(q, slot):
            rows8 = rows_of(q)
            a8, w8, b8, k8, r8 = (x[rows8, :] for x in (a_ref, w_ref, b_ref, k_ref, r_ref))
            pairs = [slice(p * LANES, (p + 1) * LANES) for p in range(N_PAIRS)]
            a_next = pltpu.roll(a8, SUBLANES - 1, 0)
            (a8_l, a8_r), (wa8_l, wa8_r) = _halves(a8), _halves(w8 * a_next)
            ba8 =jnp.concatenate([_pair_sum(b8[:, pr] * a_next[:, pr], left[0:SUBLANES]) for pr in pairs], axis=1)
            ka8 = jnp.concatenate([_pair_sum(k8[:, pr] * a_next[:, pr], left[0:SUBLANES]) for pr in pairs], axis=1)
            sp = [state[p] for p in range(N_PAIRS)]
            for i in range(0, SUBLANES, 2):
                r0, r1 = slice(i, i + 1), slice(i + 1, i + 2)
                sums = [(_pair_dot(sp[p], a8_l[r0, pairs[p]], a8_r[r0, pairs[p]], left),
                         _pair_dot(sp[p], wa8_l[r0, pairs[p]], wa8_r[r0, pairs[p]], left)) for p in range(N_PAIRS)]
                sa0, sa1 = [s[0] for s in sums], [s[1] for s in sums]
                for p in range(N_PAIRS):
                    pr = pairs[p]
                    inner = slice((p % 2) * LANES, (p % 2 + 1) * LANES)
                    vt0 = vbuf[slot, p // 2, i * HEAD:(i + 1) * HEAD, inner]
                    vt1 = vbuf[slot, p // 2, (i + 1) * HEAD:(i + 2) * HEAD, inner]
                    sa_next = sa1[p] + sa0[p] * ba8[r0, pr] + vt0 * ka8[r0, pr]
                    s1 = sp[p] * w8[r0, pr] + sa0[p] * b8[r0, pr] + vt0 * k8[r0, pr]
                    st_ref[q * SUBLANES + i, p] = s1
                    _store_tile(qbuf, slot, p, i, s1 * r8[r0, pr])
                    s2 = s1 * w8[r1, pr] + sa_next * b8[r1, pr] + vt1 * k8[r1, pr]
                    st_ref[q * SUBLANES + i + 1, p] = s2
                    _store_tile(qbuf, slot, p, i + 1, s2 * r8[r1, pr])
                    sp[p] = s2
            for p in range(N_PAIRS):
                state[p] = sp[p]

        def y_rows(q, slot):
            for g2 in range(2):
                y_ref[rows_of(q), quads[g2]] = _diag_rows(qbuf[slot, g2], diag2, bd2, sub_row2)

        v_tiles(0, 0)

        def two_groups(j, carry):
            q0 = 2 * j
            v_tiles(q0 + 1, 1)
            chain(q0, 0)
            y_rows(jnp.maximum(q0 - 1, 0), 1)
            v_tiles(jnp.minimum(q0 + 2, groups - 1), 0)
            chain(q0 + 1, 1)
            y_rows(q0, 0)
            return carry

        lax.fori_loop(0, groups // 2, two_groups, 0)
        y_rows(groups - 1, 1)
        if moves:
            moves.wait(also=(g == nb - 1))

    rows = pl.BlockSpec((tb, D_HALF), lambda g: (g, 0))
    ex_in = exchange.operands if exchange else []
    ex_out = exchange.out_shapes if exchange else []
    res = pl.pallas_call(
        body, name="wkv_fwd", grid=(nb,),
        in_specs=[rows] * 6 + [ANY] * len(ex_in),
        out_specs=[rows, pl.BlockSpec((tb, N_PAIRS, HEAD, LANES), lambda g: (g, 0, 0, 0))] + [ANY] * len(ex_out),
        out_shape=[jax.ShapeDtypeStruct((s, D_HALF), F32),
                   jax.ShapeDtypeStruct((s, N_PAIRS, HEAD, LANES), F32)] + ex_out,
        scratch_shapes=[pltpu.VMEM((N_PAIRS, HEAD, LANES), F32),
                        pltpu.VMEM((2, 2, SUBLANES * HEAD, 2 * LANES), F32),
                        pltpu.VMEM((2, 2, SUBLANES * HEAD, 2 * LANES), BF16)]
                       + (exchange.scratch() if exchange else []),
        compiler_params=_params("arbitrary"),
    )(r, w, k, a, b, v, *ex_in)
    return res[0], res[1], list(res[2:])


def _wkv_bwd(r, w, k, a, b, v, dy, st, exchange=None):
    s = r.shape[0]
    tb = SCAN_TB
    nb = s // tb

    def body(*refs):
        ((r_ref, w_ref, k_ref, a_ref, b_ref, v_ref, dy_ref, st_ref, before_ref),
         (dr_ref, dw_ref, dk_ref, dv_ref, da_ref, db_ref), (dstate, vbuf, qbuf, sbuf),
         moves) = _split_refs(refs, 9, 6, exchange)
        g = pl.program_id(0)
        first_block = g == nb - 1
        if moves:
            moves.start(also=(g == 0))

        @pl.when(g == 0)
        def _():
            dstate[...] = jnp.zeros_like(dstate)
            qbuf[...] = jnp.zeros_like(qbuf)

        left = _left_half()
        diag2, bd2 = _quad_consts()
        sub_row = lax.broadcasted_iota(jnp.int32, (SUBLANES, LANES), 0)
        sub_row2 = lax.broadcasted_iota(jnp.int32, (SUBLANES, 2 * LANES), 0)
        groups = tb // SUBLANES
        quads = [slice(g2 * 2 * LANES, (g2 + 1) * 2 * LANES) for g2 in range(2)]
        row_refs = (dr_ref, dw_ref, dk_ref, da_ref, db_ref)

        def rows_of(q):
            return pl.ds(pl.multiple_of(q * SUBLANES, SUBLANES), SUBLANES)

        def state_before(q, i, p):
            if i > 0:
                return st_ref[q * SUBLANES + i - 1, p]
            return jnp.where(q == 0, jnp.where(first_block, 0.0, before_ref[0, p]),
                             st_ref[jnp.maximum(q * SUBLANES - 1, 0), p])

        def column_tiles(q, slot):
            rows8 = rows_of(q)
            for kind, ref in enumerate((v_ref, dy_ref)):
                x8 = ref[rows8, :]
                for g2 in range(2):
                    vbuf[slot, kind, g2] = _rows_to_columns(x8[:, quads[g2]], diag2, bd2)
            a8 = a_ref[rows8, :]
            for i in range(SUBLANES):
                for p in range(N_PAIRS):
                    _store_tile(sbuf, 0, p, i, state_before(q, i, p) * a8[i:i + 1, p * LANES:(p + 1) * LANES])
            for g2 in range(2):
                vbuf[slot, 2, g2] = jnp.dot(sbuf[0, g2], bd2, preferred_element_type=F32)

        def chain(q, slot):
            rows8 = rows_of(q)
            a8, w8, b8, k8, r8 = (x[rows8, :] for x in (a_ref, w_ref, b_ref, k_ref, r_ref))
            b8_l, b8_r = _halves(b8)
            dsp = [dstate[p] for p in range(N_PAIRS)]
            outs = [[jnp.zeros((SUBLANES, LANES), F32) for _ in row_refs] for _ in range(N_PAIRS)]
            after = [st_ref[q * SUBLANES + SUBLANES - 1, p] for p in range(N_PAIRS)]
            for i in reversed(range(SUBLANES)):
                row = slice(i, i + 1)
                pl_ = [slice(p * LANES, (p + 1) * LANES) for p in range(N_PAIRS)]
                tile = [(p // 2, slice(i * HEAD, (i + 1) * HEAD), slice((p % 2) * LANES, (p % 2 + 1) * LANES))
                        for p in range(N_PAIRS)]
                sp = [state_before(q, i, p) for p in range(N_PAIRS)]
                dyt = [vbuf[(slot, 1) + tile[p]] for p in range(N_PAIRS)]
                ds = [dsp[p] + dyt[p] * r8[row, pl_[p]] for p in range(N_PAIRS)]
                dsa = [_pair_dot(ds[p], b8_l[row, pl_[p]], b8_r[row, pl_[p]], left) for p in range(N_PAIRS)]
                sa = [vbuf[(slot, 2) + tile[p]] for p in range(N_PAIRS)]
                for p in range(N_PAIRS):
                    ar, wr, br, kr = (x[row, pl_[p]] for x in (a8, w8, b8, k8))
                    vt = vbuf[(slot, 0) + tile[p]]
                    dsp[p] = ds[p] * wr + dsa[p] * ar
                    new = (_colsum(after[p] * dyt[p]), _colsum(ds[p] * sp[p]), _colsum(ds[p] * vt),
                           _colsum(sp[p] * dsa[p]), _colsum(ds[p] * sa[p]))
                    outs[p] = [jnp.where(sub_row == i, n, o) for n, o in zip(new, outs[p])]
                    _store_tile(qbuf, slot, p, i, ds[p] * kr)
                after = sp
            for p in range(N_PAIRS):
                dstate[p] = dsp[p]
                for ref, o in zip(row_refs, outs[p]):
                    ref[rows8, p * LANES:(p + 1) * LANES] = o

        def dv_rows(q, slot):
            for g2 in range(2):
                dv_ref[rows_of(q), quads[g2]] = _diag_rows(qbuf[slot, g2], diag2, bd2, sub_row2)

        column_tiles(groups - 1, 0)

        def two_groups(j, carry):
            q0 = groups - 1 - 2 * j
            column_tiles(q0 - 1, 1)
            chain(q0, 0)
            dv_rows(jnp.minimum(q0 + 1, groups - 1), 1)
            column_tiles(jnp.maximum(q0 - 2, 0), 0)
            chain(q0 - 1, 1)
            dv_rows(q0, 0)
            return carry

        lax.fori_loop(0, groups // 2, two_groups, 0)
        dv_rows(0, 1)
        if moves:
            moves.wait(also=(g == nb - 1))

    rows = pl.BlockSpec((tb, D_HALF), lambda g: (nb - 1 - g, 0))
    ex_in = exchange.operands if exchange else []
    ex_out = exchange.out_shapes if exchange else []
    res = pl.pallas_call(
        body, name="wkv_bwd", grid=(nb,),
        in_specs=[rows] * 7 + [pl.BlockSpec((tb, N_PAIRS, HEAD, LANES), lambda g: (nb - 1 - g, 0, 0, 0)),
                               pl.BlockSpec((1, N_PAIRS, HEAD, LANES),
                                            lambda g: (jnp.maximum((nb - 1 - g) * tb - 1, 0), 0, 0, 0))]
                 + [ANY] * len(ex_in),
        out_specs=[rows] * 6 + [ANY] * len(ex_out),
        out_shape=[jax.ShapeDtypeStruct((s, D_HALF), F32)] * 6 + ex_out,
        scratch_shapes=[pltpu.VMEM((N_PAIRS, HEAD, LANES), F32),
                        pltpu.VMEM((2, 3, 2, SUBLANES * HEAD, 2 * LANES), F32),
                        pltpu.VMEM((2, 2, SUBLANES * HEAD, 2 * LANES), BF16),
                        pltpu.VMEM((1, 2, SUBLANES * HEAD, 2 * LANES), BF16)]
                       + (exchange.scratch() if exchange else []),
        compiler_params=_params("arbitrary"),
    )(r, w, k, a, b, v, dy, st, st, *ex_in)
    return list(res[:6]), list(res[6:])


def _rwkv_post_math(y, r, k, v, gate, lw, lb, rk, bd):
    mean = _head_sum(y, bd) * (1.0 / HEAD)
    yc = y - mean
    var = _head_sum(yc * yc, bd) * (1.0 / HEAD)
    rstd = lax.rsqrt(var + LNX_EPS)
    yn = yc * rstd
    rkk = _head_sum(r * k * rk, bd)
    sg = _sigmoid(gate)
    pre = yn * lw + lb + rkk * v
    return yn, rstd, rkk, sg, pre


def _rwkv_post(y, r, k, v, gate, lw, lb, rk, tm=256):
    s = y.shape[0]

    def body(y_ref, r_ref, k_ref, v_ref, g_ref, lw_ref, lb_ref, rk_ref, o_ref):
        gate_v = g_ref[...]
        _, _, _, sg, pre = _rwkv_post_math(y_ref[...], r_ref[...], k_ref[...], v_ref[...], gate_v,
                                           lw_ref[...], lb_ref[...], rk_ref[...], _head_ones())
        o_ref[...] = pre * (gate_v * sg)

    blk = pl.BlockSpec((tm, D_HALF), lambda i: (i, 0))
    vec = pl.BlockSpec((1, D_HALF), lambda i: (0, 0))
    return pl.pallas_call(
        body, name="rwkv_post", grid=(s // tm,),
        in_specs=[blk] * 5 + [vec] * 3, out_specs=blk,
        out_shape=jax.ShapeDtypeStruct((s, D_HALF), F32), compiler_params=_params("parallel"),
    )(y, r, k, v, gate, lw, lb, rk)


def _rwkv_post_bwd(dmix, y, r, k, v, gate, lw, lb, rk, tm=256):
    s = y.shape[0]

    def body(dm_ref, y_ref, r_ref, k_ref, v_ref, g_ref, lw_ref, lb_ref, rk_ref,
             dy_ref, dr_ref, dk_ref, dv_ref, dg_ref, dlw_ref, dlb_ref, drk_ref):
        i = pl.program_id(0)

        @pl.when(i == 0)
        def _():
            dlw_ref[...] = jnp.zeros_like(dlw_ref)
            dlb_ref[...] = jnp.zeros_like(dlb_ref)
            drk_ref[...] = jnp.zeros_like(drk_ref)

        bd = _head_ones()
        rv, kv, vv, gate_v, lw_v, rk_v = r_ref[...], k_ref[...], v_ref[...], g_ref[...], lw_ref[...], rk_ref[...]
        yn, rstd, rkk, sg, pre = _rwkv_post_math(y_ref[...], rv, kv, vv, gate_v, lw_v, lb_ref[...], rk_v, bd)
        dm = dm_ref[...]
        dg_ref[...] = dm * pre * (sg * (1.0 + gate_v * (1.0 - sg)))
        dpre = dm * (gate_v * sg)
        dlw_ref[...] += _colsum(dpre * yn)
        dlb_ref[...] += _colsum(dpre)
        dyn = dpre * lw_v
        m1 = _head_sum(dyn, bd) * (1.0 / HEAD)
        m2 = _head_sum(dyn * yn, bd) * (1.0 / HEAD)
        dy_ref[...] = rstd * (dyn - m1 - yn * m2)
        dv_ref[...] = dpre * rkk
        drkk = _head_sum(dpre * vv, bd)
        dr_ref[...] = drkk * kv * rk_v
        dk_ref[...] = drkk * rv * rk_v
        drk_ref[...] += _colsum(drkk * rv * kv)

    blk = pl.BlockSpec((tm, D_HALF), lambda i: (i, 0))
    vec = pl.BlockSpec((1, D_HALF), lambda i: (0, 0))
    return pl.pallas_call(
        body, name="rwkv_post_bwd", grid=(s // tm,),
        in_specs=[blk] * 6 + [vec] * 3, out_specs=[blk] * 5 + [vec] * 3,
        out_shape=[jax.ShapeDtypeStruct((s, D_HALF), F32)] * 5 + [jax.ShapeDtypeStruct((1, D_HALF), F32)] * 3,
        compiler_params=_params("arbitrary"),
    )(dmix, y, r, k, v, gate, lw, lb, rk)


def _rwkv_prep_bwd(u_a, grads, mu, wl, w0, a0, kkw, kaw, tm=256):
    s = u_a.shape[0]
    nb = s // tm

    def body(ua_ref, prev_ref, drs_ref, dws_ref, dks_ref, dvs_ref, das_ref, dbs_ref, drb_ref, dkb_ref, dvb_ref,
             dgt_ref, mu_ref, wl_ref, w0_ref, a0_ref, kkw_ref, kaw_ref,
             du_ref, dmu_ref, dwl_ref, dw0_ref, da0_ref, dkkw_ref, dkaw_ref, carry):
        i = pl.program_id(0)

        @pl.when(i == 0)
        def _():
            carry[...] = jnp.zeros_like(carry)
            for ref in (dmu_ref, dwl_ref, dw0_ref, da0_ref, dkkw_ref, dkaw_ref):
                ref[...] = jnp.zeros_like(ref)

        bd = _head_ones()
        mu_v, wl_v, kkw_v, kaw_v = mu_ref[...], wl_ref[...], kkw_ref[...], kaw_ref[...]
        f = _rwkv_elementwise(ua_ref[...], prev_ref[7:8, :], i == nb - 1, mu_v, wl_v, w0_ref[...],
                              a0_ref[...], kkw_v, kaw_v, bd)
        a, kk, k0 = f["a"], f["kk"], f["k0"]
        dk = dks_ref[...] + dkb_ref[...]
        dbs = dbs_ref[...]
        dkk = dbs * a - das_ref[...]
        da = dbs * kk + dk * k0 * kaw_v
        dk0 = dk * (1.0 + (a - 1.0) * kaw_v)
        dkaw_ref[...] += _colsum(dk * k0 * (a - 1.0))
        inv = 1.0 / f["nrm"]
        proj = _head_sum(dkk * kk, bd)
        dkk0 = jnp.where(f["ss"] > 1e-24, (dkk - kk * proj) * inv, dkk * inv)
        dk0 = dk0 + dkk0 * kkw_v
        dkkw_ref[...] += _colsum(dkk0 * k0)
        dza = da * a * (1.0 - a)
        da0_ref[...] += _colsum(dza)
        dz = -dws_ref[...] * f["dec"] * f["e"] * (1.0 - f["sz"])
        dw0_ref[...] += _colsum(dz)
        dll = jnp.concatenate([dz, dza], axis=1).astype(BF16)
        dwl_ref[...] += _dot_tn(f["lin"].astype(BF16), dll)
        dlin = _dot_nt(dll, wl_v)
        lane = lax.broadcasted_iota(jnp.int32, (1, LANES), 1)
        th = f["th"]
        dlo = jnp.where(lane < LORA, dlin * (1.0 - th * th), dlin)
        dus = jnp.concatenate([drs_ref[...] + drb_ref[...], dk0, dvs_ref[...] + dvb_ref[...], dlo, dgt_ref[...]],
                              axis=1)
        dmu_ref[...] += _colsum(dus * f["delta"])
        g1 = dus * mu_v
        rows = lax.broadcasted_iota(jnp.int32, (tm, 1), 0)
        up = jnp.where(rows == tm - 1, carry[...], pltpu.roll(g1, tm - 1, 0))
        du_ref[...] = dus - g1 + up
        carry[...] = g1[0:1, :]

    rev = lambda w: pl.BlockSpec((tm, w), lambda i: (nb - 1 - i, 0))
    vec = lambda w: pl.BlockSpec((1, w), lambda i: (0, 0))
    wl_spec = pl.BlockSpec((LANES, 2 * D_HALF), lambda i: (0, 0))
    return pl.pallas_call(
        body, name="rwkv_prep_bwd", grid=(nb,),
        in_specs=[rev(SEC), pl.BlockSpec((8, SEC), lambda i: (jnp.maximum((nb - 1 - i) * (tm // 8) - 1, 0), 0))]
                 + [rev(D_HALF)] * 10 + [vec(SEC), wl_spec] + [vec(D_HALF)] * 4,
        out_specs=[rev(SEC), vec(SEC), wl_spec] + [vec(D_HALF)] * 4,
        out_shape=[jax.ShapeDtypeStruct((s, SEC), F32), jax.ShapeDtypeStruct((1, SEC), F32),
                   jax.ShapeDtypeStruct((LANES, 2 * D_HALF), F32)] + [jax.ShapeDtypeStruct((1, D_HALF), F32)] * 4,
        scratch_shapes=[pltpu.VMEM((1, SEC), F32)],
        compiler_params=_params("arbitrary"),
    )(u_a, u_a, *grads, mu, wl, w0, a0, kkw, kaw)


def _tri(tm, lower):
    r = lax.broadcasted_iota(jnp.int32, (tm, tm), 0)
    c = lax.broadcasted_iota(jnp.int32, (tm, tm), 1)
    return ((r >= c) if lower else (r <= c)).astype(BF16)


def _head_rms(x, g, bd):
    rinv = lax.rsqrt(_head_sum(x * x, bd) * (1.0 / HEAD) + RMS_EPS)
    xh = x * rinv
    return xh, rinv, xh * g


def _fox_prep(u_b, fb, qg, kg, tm=256):
    s = u_b.shape[0]

    def body(ub_ref, fb_ref, qg_ref, kg_ref, q_ref, k_ref, v_ref, cc_ref, cr_ref, carry):
        i = pl.program_id(0)

        @pl.when(i == 0)
        def _():
            carry[...] = jnp.zeros_like(carry)

        bd = _head_ones()
        _, _, qn = _head_rms(ub_ref[:, 0:512], qg_ref[...], bd)
        _, _, kn = _head_rms(ub_ref[:, 512:1024], kg_ref[...], bd)
        q_ref[...] = (qn * ATT_SCALE).astype(BF16)
        k_ref[...] = kn.astype(BF16)
        v_ref[...] = ub_ref[:, 1024:1536].astype(BF16)
        lane = lax.broadcasted_iota(jnp.int32, (1, LANES), 1)
        logf = jnp.where(lane < N_HEADS, _log_sigmoid(ub_ref[:, 2048:2176] + fb_ref[...]), 0.0)
        cum = _exact_dot(logf, _tri(tm, True), ones_first=True) + carry[...]
        for h in range(N_HEADS):
            cc_ref[h] = jnp.broadcast_to(cum[:, h:h + 1], (tm, LANES))
        cr_ref[...] = jnp.transpose(cum)[0:N_HEADS, :]
        carry[...] = cum[tm - 1:tm, :]

    blk = pl.BlockSpec((tm, D_HALF), lambda i: (i, 0))
    return pl.pallas_call(
        body, name="fox_prep", grid=(s // tm,),
        in_specs=[pl.BlockSpec((tm, SEC), lambda i: (i, 0)), pl.BlockSpec((1, LANES), lambda i: (0, 0)),
                  pl.BlockSpec((1, D_HALF), lambda i: (0, 0)), pl.BlockSpec((1, D_HALF), lambda i: (0, 0))],
        out_specs=[blk, blk, blk, pl.BlockSpec((N_HEADS, tm, LANES), lambda i: (0, i, 0)),
                   pl.BlockSpec((N_HEADS, tm), lambda i: (0, i))],
        out_shape=[jax.ShapeDtypeStruct((s, D_HALF), BF16)] * 3
                  + [jax.ShapeDtypeStruct((N_HEADS, s, LANES), F32), jax.ShapeDtypeStruct((N_HEADS, s), F32)],
        scratch_shapes=[pltpu.VMEM((1, LANES), F32)],
        compiler_params=_params("arbitrary"),
    )(u_b, fb, qg, kg)


ATT_T = 256


def _attn_fwd(q, k, v, cc, cr, u_b):
    s = q.shape[0]
    t = ATT_T
    nblk = s // t

    def body(q_ref, k_ref, v_ref, cc_ref, cr_ref, g_ref, o_ref, mix_ref, lse_ref, m_sc, l_sc, acc_sc):
        i = pl.program_id(0)
        j = pl.program_id(1)

        @pl.when(j == 0)
        def _():
            m_sc[...] = jnp.full_like(m_sc, NEG)
            l_sc[...] = jnp.zeros_like(l_sc)
            acc_sc[...] = jnp.zeros_like(acc_sc)

        def tile(on_diagonal):
            causal = _causal_tile(t) if on_diagonal else None
            left = lax.broadcasted_iota(jnp.int32, (1, LANES), 1) < HEAD
            for p in range(N_PAIRS):
                lanes = slice(p * LANES, (p + 1) * LANES)
                q2, k2, v2 = q_ref[:, lanes], k_ref[:, lanes], v_ref[:, lanes]
                acc2 = acc_sc[:, lanes]
                for e in range(2):
                    h = 2 * p + e
                    msk = left if e == 0 else jnp.logical_not(left)
                    sc = _dot_nt(jnp.where(msk, q2, jnp.zeros_like(q2)), k2)
                    sc = sc + (_wide(cc_ref[h]) - cr_ref[h:h + 1, :])
                    if on_diagonal:
                        sc = jnp.where(causal, sc, NEG)
                    m_prev = m_sc[h]
                    m_new = jnp.maximum(m_prev, jnp.max(sc, axis=1, keepdims=True))
                    alpha = jnp.exp(m_prev - m_new)
                    pm = jnp.exp(sc - _wide(m_new))
                    l_sc[h] = alpha * l_sc[h] + jnp.sum(pm, axis=1, keepdims=True)
                    m_sc[h] = m_new
                    pv = jnp.dot(pm.astype(BF16), v2, preferred_element_type=F32)
                    acc2 = jnp.where(msk, alpha * acc2 + pv, acc2)
                acc_sc[:, lanes] = acc2

        pl.when(j < i)(functools.partial(tile, False))
        pl.when(j == i)(functools.partial(tile, True))

        @pl.when(j == i)
        def _():
            left = lax.broadcasted_iota(jnp.int32, (1, LANES), 1) < HEAD
            for p in range(N_PAIRS):
                lanes = slice(p * LANES, (p + 1) * LANES)
                inv = jnp.where(left, 1.0 / l_sc[2 * p], 1.0 / l_sc[2 * p + 1])
                o = acc_sc[:, lanes] * inv
                o_ref[:, lanes] = o
                gate = g_ref[:, lanes]
                mix_ref[:, lanes] = o * (gate * _sigmoid(gate))
            for h in range(N_HEADS):
                lse_ref[h] = m_sc[h] + jnp.log(l_sc[h])

    qblk = pl.BlockSpec((t, D_HALF), lambda i, j: (i, 0))
    kblk = pl.BlockSpec((t, D_HALF), lambda i, j: (jnp.minimum(i, j), 0))
    return pl.pallas_call(
        body, name="fox_attn_fwd", grid=(nblk, nblk),
        in_specs=[qblk, kblk, kblk, pl.BlockSpec((N_HEADS, t, LANES), lambda i, j: (0, i, 0)),
                  pl.BlockSpec((N_HEADS, t), lambda i, j: (0, jnp.minimum(i, j))),
                  pl.BlockSpec((t, D_HALF), lambda i, j: (i, 3))],
        out_specs=[qblk, qblk, pl.BlockSpec((N_HEADS, t, LANES), lambda i, j: (0, i, 0))],
        out_shape=[jax.ShapeDtypeStruct((s, D_HALF), F32), jax.ShapeDtypeStruct((s, D_HALF), F32),
                   jax.ShapeDtypeStruct((N_HEADS, s, LANES), F32)],
        scratch_shapes=[pltpu.VMEM((N_HEADS, t, LANES), F32), pltpu.VMEM((N_HEADS, t, LANES), F32),
                        pltpu.VMEM((t, D_HALF), F32)],
        compiler_params=_params("parallel", "arbitrary"),
    )(q, k, v, cc, cr, u_b)


def _fox_post_bwd(dmix, o, u_b, tm=256):
    s = o.shape[0]

    def body(dm_ref, o_ref, g_ref, do_ref, dg_ref):
        gate = g_ref[...]
        sg = _sigmoid(gate)
        dm = dm_ref[...]
        do_ref[...] = (dm * (gate * sg)).astype(BF16)
        dg_ref[...] = dm * o_ref[...] * (sg * (1.0 + gate * (1.0 - sg)))

    blk = pl.BlockSpec((tm, D_HALF), lambda i: (i, 0))
    return pl.pallas_call(
        body, name="fox_post_bwd", grid=(s // tm,),
        in_specs=[blk, blk, pl.BlockSpec((tm, D_HALF), lambda i: (i, 3))], out_specs=[blk] * 2,
        out_shape=[jax.ShapeDtypeStruct((s, D_HALF), BF16), jax.ShapeDtypeStruct((s, D_HALF), F32)],
        compiler_params=_params("parallel"),
    )(dmix, o, u_b)


def _causal_tile(t):
    return lax.broadcasted_iota(jnp.int32, (t, t), 0) >= lax.broadcasted_iota(jnp.int32, (t, t), 1)


def _wide(x):
    return jnp.concatenate([x, x], axis=1)


def _attn_probs(q2, k2, v2, do2, msk, causal, bias, lse_rows):
    zero = jnp.zeros_like(q2)
    qh = jnp.where(msk, q2, zero)
    doh = jnp.where(msk, do2, zero)
    sc = _dot_nt(qh, k2) + bias
    if causal is not None:
        sc = jnp.where(causal, sc, NEG)
    pm = jnp.exp(sc - _wide(lse_rows))
    dp = _dot_nt(doh, v2)
    return qh, doh, pm, dp


def _attn_bwd_rowdot(q, k, v, do, lse, cc, cr):
    s = q.shape[0]
    t = ATT_T
    nblk = s // t

    def body(q_ref, k_ref, v_ref, do_ref, lse_ref, cc_ref, cr_ref, dd_ref, acc):
        i = pl.program_id(0)
        j = pl.program_id(1)

        @pl.when(j == 0)
        def _():
            acc[...] = jnp.zeros_like(acc)

        def tile(on_diagonal):
            causal = _causal_tile(t) if on_diagonal else None
            left = lax.broadcasted_iota(jnp.int32, (1, LANES), 1) < HEAD
            for p in range(N_PAIRS):
                lanes = slice(p * LANES, (p + 1) * LANES)
                q2, k2, v2, do2 = q_ref[:, lanes], k_ref[:, lanes], v_ref[:, lanes], do_ref[:, lanes]
                for e in range(2):
                    h = 2 * p + e
                    msk = left if e == 0 else jnp.logical_not(left)
                    bias = _wide(cc_ref[h]) - cr_ref[h:h + 1, :]
                    _, _, pm, dp = _attn_probs(q2, k2, v2, do2, msk, causal, bias, lse_ref[h])
                    acc[h] += jnp.sum(pm * dp, axis=1, keepdims=True)

        pl.when(j < i)(functools.partial(tile, False))
        pl.when(j == i)(functools.partial(tile, True))

        @pl.when(j == i)
        def _():
            dd_ref[...] = acc[...]

    qblk = pl.BlockSpec((t, D_HALF), lambda i, j: (i, 0))
    qcol = pl.BlockSpec((N_HEADS, t, LANES), lambda i, j: (0, i, 0))
    kblk = pl.BlockSpec((t, D_HALF), lambda i, j: (jnp.minimum(i, j), 0))
    return pl.pallas_call(
        body, name="fox_attn_rowdot", grid=(nblk, nblk),
        in_specs=[qblk, kblk, kblk, qblk, qcol, qcol, pl.BlockSpec((N_HEADS, t), lambda i, j: (0, jnp.minimum(i, j)))],
        out_specs=qcol, out_shape=jax.ShapeDtypeStruct((N_HEADS, s, LANES), F32),
        scratch_shapes=[pltpu.VMEM((N_HEADS, t, LANES), F32)],
        compiler_params=_params("parallel", "arbitrary"),
    )(q, k, v, do, lse, cc, cr)


def _attn_bwd(q, k, v, do, lse, dd, cc, cr):
    s = q.shape[0]
    t = ATT_T
    nblk = s // t

    def body(q_ref, k_ref, v_ref, do_ref, lse_ref, dd_ref, cc_ref, cr_ref,
             dq_ref, dk_ref, dv_ref, dcr_ref, dk_sc, dv_sc, dcr_sc):
        j = pl.program_id(0)
        i = pl.program_id(1)

        @pl.when(jnp.logical_and(j == 0, i == 0))
        def _():
            dq_ref[...] = jnp.zeros_like(dq_ref)

        @pl.when(i == 0)
        def _():
            dk_sc[...] = jnp.zeros_like(dk_sc)
            dv_sc[...] = jnp.zeros_like(dv_sc)
            dcr_sc[...] = jnp.zeros_like(dcr_sc)

        def tile(on_diagonal):
            causal = _causal_tile(t) if on_diagonal else None
            left = lax.broadcasted_iota(jnp.int32, (1, LANES), 1) < HEAD
            qrows = pl.ds(pl.multiple_of(i * t, t), t)
            for p in range(N_PAIRS):
                lanes = slice(p * LANES, (p + 1) * LANES)
                q2, k2, v2, do2 = q_ref[:, lanes], k_ref[:, lanes], v_ref[:, lanes], do_ref[:, lanes]
                zero = jnp.zeros_like(q2)
                dq2 = jnp.zeros((t, LANES), F32)
                dk2 = jnp.zeros((t, LANES), F32)
                dv2 = jnp.zeros((t, LANES), F32)
                for e in range(2):
                    h = 2 * p + e
                    msk = left if e == 0 else jnp.logical_not(left)
                    bias = _wide(cc_ref[h]) - cr_ref[h:h + 1, :]
                    qh, doh, pm, dp = _attn_probs(q2, k2, v2, do2, msk, causal, bias, lse_ref[h])
                    dsc = pm * (dp - _wide(dd_ref[h]))
                    dsb = dsc.astype(BF16)
                    dv2 += _dot_tn(pm.astype(BF16), doh)
                    dk2 += _dot_tn(dsb, qh)
                    dq2 += jnp.dot(dsb, jnp.where(msk, k2, zero), preferred_element_type=F32)
                    dcr_sc[h:h + 1, :] += -_colsum(dsc)
                dq_ref[qrows, lanes] += dq2 * ATT_SCALE
                dk_sc[:, lanes] += dk2
                dv_sc[:, lanes] += dv2

        pl.when(i > j)(functools.partial(tile, False))
        pl.when(i == j)(functools.partial(tile, True))

        @pl.when(i == nblk - 1)
        def _():
            dk_ref[...] = dk_sc[...]
            dv_ref[...] = dv_sc[...]
            dcr_ref[...] = dcr_sc[...]

    qblk = pl.BlockSpec((t, D_HALF), lambda j, i: (jnp.maximum(i, j), 0))
    qcol = pl.BlockSpec((N_HEADS, t, LANES), lambda j, i: (0, jnp.maximum(i, j), 0))
    kblk = pl.BlockSpec((t, D_HALF), lambda j, i: (j, 0))
    return pl.pallas_call(
        body, name="fox_attn_bwd", grid=(nblk, nblk),
        in_specs=[qblk, kblk, kblk, qblk, qcol, qcol, qcol, pl.BlockSpec((N_HEADS, t), lambda j, i: (0, j))],
        out_specs=[pl.BlockSpec((s, D_HALF), lambda j, i: (0, 0)), kblk, kblk,
                   pl.BlockSpec((N_HEADS, t), lambda j, i: (0, j))],
        out_shape=[jax.ShapeDtypeStruct((s, D_HALF), F32)] * 3 + [jax.ShapeDtypeStruct((N_HEADS, s), F32)],
        scratch_shapes=[pltpu.VMEM((t, D_HALF), F32), pltpu.VMEM((t, D_HALF), F32), pltpu.VMEM((N_HEADS, t), F32)],
        compiler_params=_params("arbitrary", "arbitrary"),
    )(q, k, v, do, lse, dd, cc, cr)


def _fox_prep_bwd(u_b, dq, dk, dv, dgate, dcum, fb, qg, kg, tm=256):
    s = u_b.shape[0]
    nb = s // tm

    def body(ub_ref, dq_ref, dk_ref, dv_ref, dg_ref, dc_ref, fb_ref, qg_ref, kg_ref,
             du_ref, dqg_ref, dkg_ref, dfb_ref, carry):
        i = pl.program_id(0)

        @pl.when(i == 0)
        def _():
            carry[...] = jnp.zeros_like(carry)
            dqg_ref[...] = jnp.zeros_like(dqg_ref)
            dkg_ref[...] = jnp.zeros_like(dkg_ref)
            dfb_ref[...] = jnp.zeros_like(dfb_ref)

        bd = _head_ones()
        for lo, g_ref, d_ref, dgain_ref in ((0, qg_ref, dq_ref, dqg_ref), (512, kg_ref, dk_ref, dkg_ref)):
            gain = g_ref[...]
            xh, rinv, _ = _head_rms(ub_ref[:, lo:lo + 512], gain, bd)
            dn = d_ref[...]
            dgain_ref[...] += _colsum(dn * xh)
            dxh = dn * gain
            du_ref[:, lo:lo + 512] = rinv * (dxh - xh * (_head_sum(dxh * xh, bd) * (1.0 / HEAD)))
        du_ref[:, 1024:1536] = dv_ref[...]
        du_ref[:, 1536:2048] = dg_ref[...]
        lane = lax.broadcasted_iota(jnp.int32, (1, LANES), 1)
        dc = dc_ref[...]
        dlogf = _exact_dot(dc, _tri(tm, False), ones_first=True) + carry[...]
        carry[...] += _colsum(dc)
        fl = ub_ref[:, 2048:2176] + fb_ref[...]
        dfl = jnp.where(lane < N_HEADS, dlogf * (1.0 - _sigmoid(fl)), 0.0)
        du_ref[:, 2048:2176] = dfl
        dfb_ref[...] += _colsum(dfl)

    rev = lambda w: pl.BlockSpec((tm, w), lambda i: (nb - 1 - i, 0))
    vec = lambda w: pl.BlockSpec((1, w), lambda i: (0, 0))
    return pl.pallas_call(
        body, name="fox_prep_bwd", grid=(nb,),
        in_specs=[rev(SEC)] + [rev(D_HALF)] * 4 + [rev(LANES), vec(LANES), vec(D_HALF), vec(D_HALF)],
        out_specs=[rev(SEC), vec(D_HALF), vec(D_HALF), vec(LANES)],
        out_shape=[jax.ShapeDtypeStruct((s, SEC), F32), jax.ShapeDtypeStruct((1, D_HALF), F32),
                   jax.ShapeDtypeStruct((1, D_HALF), F32), jax.ShapeDtypeStruct((1, LANES), F32)],
        scratch_shapes=[pltpu.VMEM((1, LANES), F32)],
        compiler_params=_params("arbitrary"),
    )(u_b, dq, dk, dv, dgate, dcum, fb, qg, kg)


def _merge(mix_a, mix_b, u_g, x, tgt, wa, wb, wo, fg, tm=256):
    s, d = x.shape

    def body(ma_ref, mb_ref, ug_ref, x_ref, t_ref, wa_ref, wb_ref, wo_ref, fg_ref,
             dx2_ref, dma_ref, dmb_ref, dug_ref, dwa_ref, dwb_ref, dwo_ref, dfg_ref, loss_ref):
        i = pl.program_id(0)

        @pl.when(i == 0)
        def _():
            for ref in (dwa_ref, dwb_ref, dwo_ref, dfg_ref, loss_ref):
                ref[...] = jnp.zeros_like(ref)

        wa_v, wb_v, wo_v, fg_v = wa_ref[...], wb_ref[...], wo_ref[...], fg_ref[...]
        ma = ma_ref[...].astype(BF16)
        mb = mb_ref[...].astype(BF16)
        ya = jnp.dot(ma, wa_v, preferred_element_type=F32)
        yb = jnp.dot(mb, wb_v, preferred_element_type=F32)
        sa = _sigmoid(ug_ref[:, 0:d])
        sb = _sigmoid(ug_ref[:, d:2 * d])
        merged = (sa * ya + sb * yb).astype(BF16)
        x2 = x_ref[...] + jnp.dot(merged, wo_v, preferred_element_type=F32)
        r2 = lax.rsqrt(jnp.mean(x2 * x2, axis=-1, keepdims=True) + RMS_EPS)
        x2h = x2 * r2
        err = x2h * fg_v - t_ref[...]
        loss_ref[...] += _colsum(err * err)
        dy = err * (1.0 / d)
        dfg_ref[...] += _colsum(dy * x2h)
        dx2h = dy * fg_v
        dx2 = r2 * (dx2h - x2h * jnp.mean(dx2h * x2h, axis=-1, keepdims=True))
        dx2_ref[...] = dx2
        dx2b = dx2.astype(BF16)
        dmerged = _dot_nt(dx2b, wo_v)
        dwo_ref[...] += _dot_tn(merged, dx2b)
        dya = dmerged * sa
        dyb = dmerged * sb
        dug_ref[:, 0:d] = dya * ya * (1.0 - sa)
        dug_ref[:, d:2 * d] = dyb * yb * (1.0 - sb)
        dyab = dya.astype(BF16)
        dybb = dyb.astype(BF16)
        dma_ref[...] = _dot_nt(dyab, wa_v)
        dmb_ref[...] = _dot_nt(dybb, wb_v)
        dwa_ref[...] += _dot_tn(ma, dyab)
        dwb_ref[...] += _dot_tn(mb, dybb)

    row = lambda w: pl.BlockSpec((tm, w), lambda i: (i, 0))
    full = lambda a: pl.BlockSpec(a.shape, lambda i: (0, 0))
    fshape = lambda a: jax.ShapeDtypeStruct(a.shape, F32)
    return pl.pallas_call(
        body, name="merge_fwd_bwd", grid=(s // tm,),
        in_specs=[row(D_HALF), row(D_HALF), row(GATE_COLS), row(d), row(d), full(wa), full(wb), full(wo), full(fg)],
        out_specs=[row(d), row(D_HALF), row(D_HALF), row(GATE_COLS), full(wa), full(wb), full(wo), full(fg), full(fg)],
        out_shape=[jax.ShapeDtypeStruct((s, d), F32), jax.ShapeDtypeStruct((s, D_HALF), F32),
                   jax.ShapeDtypeStruct((s, D_HALF), F32), jax.ShapeDtypeStruct((s, GATE_COLS), F32),
                   fshape(wa), fshape(wb), fshape(wo), fshape(fg), fshape(fg)],
        compiler_params=_params("arbitrary"),
    )(mix_a, mix_b, u_g, x, tgt, wa, wb, wo, fg)


def _lora_weight(w_up, a_up):
    z = jnp.zeros((LORA, D_HALF), w_up.dtype)
    return jnp.concatenate([jnp.concatenate([w_up, z], axis=1), jnp.concatenate([z, a_up], axis=1)], axis=0)


def _device_grads(x, tgt, p, w_a, w_up, a_up, late_weights, fwd_exchange=None, bwd_exchange=None, tail_exchange=None):
    wl = _lora_weight(w_up, a_up)
    rk = p["r_k"].reshape(1, D_HALF)
    fb = jnp.pad(p["f_bias"], ((0, 0), (0, LANES - N_HEADS)))
    qg = jnp.tile(p["q_norm_g"], (1, N_HEADS))
    kg = jnp.tile(p["k_norm_g"], (1, N_HEADS))
    fg = p["final_norm_g"].reshape(1, D_MODEL)
    mixer = (p["shift_mu"], wl, p["w0"], p["a0"], p["k_k"], p["k_a"])

    h = _rmsnorm_in(x, p["norm_g"])
    u_a = _matmul_nn(h, w_a, "inproj_rwkv")
    r, dec, k, v, av, bv, gate_a = _rwkv_prep(u_a, *mixer)
    y, st, arrived = _wkv_fwd(r, dec, k, av, bv, v, fwd_exchange)
    mix_a = _rwkv_post(y, r, k, v, gate_a, p["lnx_w"], p["lnx_b"], rk)

    w_b, w_g, w_out_a, w_out_b, w_out = late_weights(arrived)
    u_b = _matmul_nn(h, w_b, "inproj_fox")
    u_g = _matmul_nn(h, w_g, "inproj_gate")
    q, kn, vb, cc, cr = _fox_prep(u_b, fb, qg, kg)
    o, mix_b, lse = _attn_fwd(q, kn, vb, cc, cr, u_b)

    dx2, dmix_a, dmix_b, du_g, dwa, dwb, dwo, dfg, loss_vec = _merge(
        mix_a, mix_b, u_g, x, tgt, w_out_a, w_out_b, w_out, fg)

    do, dgate_b = _fox_post_bwd(dmix_b, o, u_b)
    dd = _attn_bwd_rowdot(q, kn, vb, do, lse, cc, cr)
    dq, dk_att, dv_att, dcr = _attn_bwd(q, kn, vb, do, lse, dd, cc, cr)
    dcum = jnp.pad(dcr.T, ((0, 0), (0, LANES - N_HEADS)))
    du_b, dqg, dkg, dfb = _fox_prep_bwd(u_b, dq, dk_att, dv_att, dgate_b, dcum, fb, qg, kg)
    h_t = h.T
    dw_b = _matmul_tn_acc(h_t, du_b, "dw_fox")
    dw_g = _matmul_tn_acc(h_t, du_g, "dw_gate")

    dy, dr_b, dk_b, dv_b, dgate_a, dlw, dlb, drk = _rwkv_post_bwd(
        dmix_a, y, r, k, v, gate_a, p["lnx_w"], p["lnx_b"], rk)
    scan_grads, sent = _wkv_bwd(r, dec, k, av, bv, v, dy, st,
                                bwd_exchange(dw_b, dw_g, dwa, dwb, dwo) if bwd_exchange else None)
    du_a, dmu, dwl, dw0, da0, dkkw, dkaw = _rwkv_prep_bwd(u_a, (*scan_grads, dr_b, dk_b, dv_b, dgate_a), *mixer)
    dw_a = _matmul_tn_acc(h_t, du_a, "dw_rwkv")
    dw_up, da_up = dwl[:LORA, :D_HALF], dwl[LORA:, D_HALF:]
    sent_last = _run_on_sequencer(tail_exchange(dw_a, dw_up, da_up), "scatter_tail", 1) if tail_exchange else []
    grad_x, dnorm_g, _ = _inproj_bwd(du_a, du_b, du_g, w_a, w_b, w_g, x, dx2, p["norm_g"])

    grads = dict(
        norm_g=dnorm_g, w_in=(dw_a, dw_b, dw_g), shift_mu=dmu,
        w_lora_up=dw_up, w0=dw0, a_lora_up=da_up, a0=da0, k_k=dkkw, k_a=dkaw,
        r_k=drk.reshape(1, N_HEADS, HEAD), lnx_w=dlw, lnx_b=dlb, f_bias=dfb[:, :N_HEADS],
        q_norm_g=dqg.reshape(N_HEADS, HEAD).sum(axis=0, keepdims=True),
        k_norm_g=dkg.reshape(N_HEADS, HEAD).sum(axis=0, keepdims=True),
        w_out_a=dwa, w_out_b=dwb, w_out=dwo, final_norm_g=dfg.reshape(D_MODEL))
    return loss_vec, grad_x, grads, sent, sent_last


CHIP_FLIPS = ((1, 0), (0, 1), (1, 1))
ANY = pl.BlockSpec(memory_space=pl.ANY)


def _position():
    return lax.axis_index("x"), lax.axis_index("y"), lax.axis_index("c")


def _flip(v, f):
    return 1 - v if f else v


def _both(a, b):
    if a is None:
        return b
    return a if b is None else jnp.logical_and(a, b)


def _when(cond, fn):
    if cond is None:
        fn()
    else:
        pl.when(cond)(fn)


class _Moves:
    def __init__(self, send_sems, recv_sems, local_sems):
        self.send_sems, self.recv_sems, self.local_sems = send_sems, recv_sems, local_sems
        self.remote, self.local = [], []

    def send(self, src, dst, peer, landing, send_if=None, recv_if=None):
        k = len(self.remote)
        sems = dict(send_sem=self.send_sems.at[k], recv_sem=self.recv_sems.at[k], device_id=peer, device_id_type=MESH)
        out = pltpu.make_async_remote_copy(src_ref=src, dst_ref=dst, **sems)
        arrival = pltpu.make_async_remote_copy(src_ref=src, dst_ref=landing, **sems)
        self.remote.append((out, arrival, send_if, recv_if))

    def copy(self, src, dst, cond=None):
        cp = pltpu.make_async_copy(src, dst, self.local_sems.at[len(self.local)])
        self.local.append((cp, cond))

    def start(self, also=None):
        for cp, cond in self.local:
            _when(_both(also, cond), cp.start)
        for out, _, send_if, _ in self.remote:
            _when(_both(also, send_if), out.start)

    def wait_arrivals(self, also=None):
        for _, arrival, _, recv_if in self.remote:
            _when(_both(also, recv_if), arrival.wait_recv)

    def wait_sent(self, also=None):
        for out, _, send_if, _ in self.remote:
            _when(_both(also, send_if), out.wait_send)
        for cp, cond in self.local:
            _when(_both(also, cond), cp.wait)

    def wait(self, also=None):
        self.wait_arrivals(also)
        self.wait_sent(also)


class _Exchange:
    def __init__(self, operands, out_shapes, n_remote, n_local, build, n_relay=0, relay=None):
        self.operands, self.out_shapes = list(operands), list(out_shapes)
        self.n_remote, self.n_local, self.build = n_remote, n_local, build
        self.n_relay, self.relay = n_relay, relay

    def scratch(self):
        return [pltpu.SemaphoreType.DMA((self.n_remote,)), pltpu.SemaphoreType.DMA((self.n_remote,)),
                pltpu.SemaphoreType.DMA((max(self.n_local, 1),))]

    def moves(self, in_refs, out_refs, sems):
        mv = _Moves(*sems)
        self.build(mv, in_refs, out_refs)
        return mv

    def run_alone(self, name):
        n_in, n_out = len(self.operands), len(self.out_shapes)
        relay_scratch = [pltpu.SemaphoreType.DMA((self.n_relay,))] * 2 if self.relay else []

        def body(*refs):
            ins, outs, sems = refs[:n_in], refs[n_in:n_in + n_out], refs[n_in + n_out:]
            mv = self.moves(ins, outs, sems[:3])
            mv.start()
            mv.wait_arrivals()
            if self.relay:
                passed = _Moves(sems[3], sems[4], None)
                self.relay(passed, ins, outs)
                passed.start()
                passed.wait()
            mv.wait_sent()

        return pl.pallas_call(
            body, name=name, in_specs=[ANY] * n_in, out_specs=[ANY] * n_out, out_shape=self.out_shapes,
            scratch_shapes=self.scratch() + relay_scratch, compiler_params=pltpu.CompilerParams(has_side_effects=True),
        )(*self.operands)


def _run_on_sequencer(exchange, name, collective_id):
    ins = [jax.new_ref(a, memory_space=pltpu.MemorySpace.HBM) for a in exchange.operands]
    outs = [jax.empty_ref(s, memory_space=pltpu.MemorySpace.HBM) for s in exchange.out_shapes]
    relay_scratch = [pltpu.SemaphoreType.DMA((exchange.n_relay,))] * 2 if exchange.relay else []

    def launch(*sems):
        x, y, c = _position()
        peers = [(_flip(x, fx), _flip(y, fy), c) for fx, fy in CHIP_FLIPS] + ([(x, y, 1 - c)] if exchange.relay else [])
        barrier = pltpu.get_barrier_semaphore()
        for peer in peers:
            pl.semaphore_signal(barrier, inc=1, device_id=peer, device_id_type=MESH)
        pl.semaphore_wait(barrier, len(peers))
        moves = exchange.moves(ins, outs, sems[:3])
        moves.start()
        moves.wait_arrivals()
        if exchange.relay:
            passed = _Moves(sems[3], sems[4], None)
            exchange.relay(passed, ins, outs)
            passed.start()
            passed.wait()
        moves.wait_sent()

    pl.kernel(launch, mesh=plsc.ScalarSubcoreMesh(axis_name="sequencer", num_cores=1), name=name,
              scratch_types=tuple(exchange.scratch() + relay_scratch),
              compiler_params=pltpu.CompilerParams(collective_id=collective_id))()
    return [o[...] for o in outs]


def _row_major_copy(a, name):
    r, c = a.shape
    tr = _row_tile(r)

    def body(a_ref, o_ref):
        o_ref[...] = a_ref[...]

    blk = pl.BlockSpec((tr, c), lambda i: (i, 0))
    return pl.pallas_call(body, name=name, grid=(r // tr,), in_specs=[blk], out_specs=blk,
                          out_shape=jax.ShapeDtypeStruct(a.shape, a.dtype), compiler_params=_params("parallel"))(a)


def _is_chip(x, y, chip):
    return jnp.logical_and(x == chip // 2, y == chip % 2)


def _gather_exchange(from_chip, from_all, split=()):
    n1, n2 = len(from_chip), len(from_all)

    def rows_of(t, c):
        half = from_chip[t][1].shape[0] // 2
        return pl.ds(c * half, half)

    def build(mv, ins, outs):
        x, y, c = _position()
        me = 2 * x + y
        for t, (chip, _) in enumerate(from_chip):
            mv.copy(ins[t], outs[t], cond=_is_chip(x, y, chip))
        for t in range(n2):
            mv.copy(ins[n1 + t], outs[n1 + t].at[me])
        for fx, fy in CHIP_FLIPS:
            px, py = _flip(x, fx), _flip(y, fy)
            peer = (px, py, c)
            for t, (chip, _) in enumerate(from_chip):
                part = rows_of(t, c) if t in split else slice(None)
                mv.send(ins[t].at[part], outs[t].at[part], peer, landing=outs[t].at[part],
                        send_if=_is_chip(x, y, chip), recv_if=_is_chip(px, py, chip))
            for t in range(n2):
                mv.send(ins[n1 + t], outs[n1 + t].at[me], peer, landing=outs[n1 + t].at[2 * px + py])

    def relay(mv, ins, outs):
        x, y, c = _position()
        for t in split:
            came = jnp.logical_not(_is_chip(x, y, from_chip[t][0]))
            mv.send(outs[t].at[rows_of(t, c)], outs[t].at[rows_of(t, c)], (x, y, 1 - c),
                    landing=outs[t].at[rows_of(t, 1 - c)], send_if=came, recv_if=came)

    arrays = [a for _, a in from_chip] + list(from_all)
    shapes = [jax.ShapeDtypeStruct(a.shape, a.dtype) for _, a in from_chip]
    shapes += [jax.ShapeDtypeStruct((N_CHIPS,) + a.shape, a.dtype) for a in from_all]
    return _Exchange(arrays, shapes, len(CHIP_FLIPS) * (n1 + n2), n1 + n2, build,
                     n_relay=len(split), relay=relay if split else None)


def _scatter_exchange(to_chip, to_all):
    n1, n2 = len(to_chip), len(to_all)

    def build(mv, ins, outs):
        x, y, c = _position()
        for f, (fx, fy) in enumerate(CHIP_FLIPS):
            px, py = _flip(x, fx), _flip(y, fy)
            peer = (px, py, c)
            for t, (chip, _) in enumerate(to_chip):
                mv.send(ins[t], outs[t].at[f], peer, landing=outs[t].at[f],
                        send_if=_is_chip(px, py, chip), recv_if=_is_chip(x, y, chip))
            for t in range(n2):
                mv.send(ins[n1 + t].at[2 * px + py], outs[n1 + t].at[f], peer, landing=outs[n1 + t].at[f])

    arrays = [a for _, a in to_chip] + list(to_all)
    shapes = [jax.ShapeDtypeStruct((len(CHIP_FLIPS),) + a.shape, a.dtype) for _, a in to_chip]
    shapes += [jax.ShapeDtypeStruct((len(CHIP_FLIPS),) + a.shape[1:], a.dtype) for a in to_all]
    return _Exchange(arrays, shapes, len(CHIP_FLIPS) * (n1 + n2), 0, build)


def _swap_sibling(tensors, name):
    n = len(tensors)

    def body(*refs):
        ins, outs = refs[:n], refs[n:2 * n]
        send_sems, recv_sems = refs[2 * n:]
        x, y, c = _position()
        copies = [pltpu.make_async_remote_copy(
            src_ref=ins[t], dst_ref=outs[t], send_sem=send_sems.at[t], recv_sem=recv_sems.at[t],
            device_id=(x, y, 1 - c), device_id_type=MESH) for t in range(n)]
        for cp in copies:
            cp.start()
        for cp in copies:
            cp.wait_recv()
        for cp in copies:
            cp.wait_send()

    return pl.pallas_call(
        body, name=name, in_specs=[ANY] * n, out_specs=[ANY] * n,
        out_shape=[jax.ShapeDtypeStruct(a.shape, a.dtype) for a in tensors],
        scratch_shapes=[pltpu.SemaphoreType.DMA((n,)), pltpu.SemaphoreType.DMA((n,))],
        compiler_params=pltpu.CompilerParams(has_side_effects=True),
    )(*tensors)


def _allreduce_small(slab):
    stages = 3

    def body(x_ref, o_ref, buf, send_sems, recv_sems):
        x, y, c = _position()
        peers = ((1 - x, y, c), (x, 1 - y, c), (x, y, 1 - c))
        o_ref[...] = x_ref[...]
        for k, peer in enumerate(peers):
            cp = pltpu.make_async_remote_copy(src_ref=o_ref, dst_ref=buf.at[k], send_sem=send_sems.at[k],
                                              recv_sem=recv_sems.at[k], device_id=peer, device_id_type=MESH)
            cp.start()
            cp.wait()
            o_ref[...] = o_ref[...] + buf[k]

    return pl.pallas_call(
        body, name="allreduce_small",
        in_specs=[pl.BlockSpec(memory_space=pltpu.VMEM)], out_specs=pl.BlockSpec(memory_space=pltpu.VMEM),
        out_shape=jax.ShapeDtypeStruct(slab.shape, slab.dtype),
        scratch_shapes=[pltpu.VMEM((stages,) + slab.shape, slab.dtype),
                        pltpu.SemaphoreType.DMA((stages,)), pltpu.SemaphoreType.DMA((stages,))],
        compiler_params=pltpu.CompilerParams(has_side_effects=True),
    )(slab)


def _row_tile(r):
    return min(r, 256)


def _sum4(stack, recv, me):
    _, r, c = stack.shape
    tr = _row_tile(r)

    def body(me_ref, own_ref, recv_ref, o_ref):
        o_ref[...] = (((own_ref[...] + recv_ref[0].astype(F32)) + recv_ref[1].astype(F32))
                      + recv_ref[2].astype(F32))

    return pl.pallas_call(
        body, name="sum_partials",
        grid_spec=pltpu.PrefetchScalarGridSpec(
            num_scalar_prefetch=1, grid=(r // tr,),
            in_specs=[pl.BlockSpec((None, tr, c), lambda i, me_ref: (me_ref[0], i, 0)),
                      pl.BlockSpec((len(CHIP_FLIPS), tr, c), lambda i, me_ref: (0, i, 0))],
            out_specs=pl.BlockSpec((tr, c), lambda i, me_ref: (i, 0))),
        out_shape=jax.ShapeDtypeStruct((r, c), F32), compiler_params=_params("parallel"),
    )(me, stack, recv)


def _sum_block(own, recv):
    r, c = own.shape
    tr = _row_tile(r)

    def body(own_ref, recv_ref, o_ref):
        o_ref[...] = (((own_ref[...] + recv_ref[0].astype(F32)) + recv_ref[1].astype(F32))
                      + recv_ref[2].astype(F32))

    return pl.pallas_call(
        body, name="sum_block", grid=(r // tr,),
        in_specs=[pl.BlockSpec((tr, c), lambda i: (i, 0)), pl.BlockSpec((len(CHIP_FLIPS), tr, c), lambda i: (0, i, 0))],
        out_specs=pl.BlockSpec((tr, c), lambda i: (i, 0)),
        out_shape=jax.ShapeDtypeStruct((r, c), F32), compiler_params=_params("parallel"),
    )(own, recv)


def _adamw_math(w, g, m, v):
    m = ADAM_B1 * m + (1.0 - ADAM_B1) * g
    v = ADAM_B2 * v + (1.0 - ADAM_B2) * (g * g)
    m_hat = m / (1.0 - ADAM_B1 ** ADAM_STEP)
    v_hat = v / (1.0 - ADAM_B2 ** ADAM_STEP)
    delta = -ADAM_LR * (m_hat / (jnp.sqrt(v_hat) + ADAM_EPS) + ADAM_WD * w)
    return delta, m, v


def _adamw(w, m, v, g_parts, name):
    r, c = w.shape
    tr = _row_tile(r)
    n = len(g_parts)

    def body(*refs):
        w_ref, m_ref, v_ref = refs[:3]
        g_refs = refs[3:3 + n]
        g_out, d_out, m_out, v_out = refs[3 + n:]
        g = g_refs[0][...]
        for ref in g_refs[1:]:
            g = g + ref[...]
        g_out[...] = g
        d_out[...], m_out[...], v_out[...] = _adamw_math(w_ref[...], g, m_ref[...], v_ref[...])

    blk = pl.BlockSpec((tr, c), lambda i: (i, 0))
    return pl.pallas_call(
        body, name=name, grid=(r // tr,), in_specs=[blk] * (3 + n), out_specs=[blk] * 4,
        out_shape=[jax.ShapeDtypeStruct((r, c), F32)] * 4, compiler_params=_params("parallel"),
    )(w, m, v, *g_parts)


def _adamw_small(total, w, m, v):
    sizes = [w[n].size for n in SMALL]
    flat = lambda d: [d[n].reshape(1, -1) for n in SMALL]
    k = len(SMALL)

    def body(*refs):
        total_ref, w_refs, m_refs, v_refs = refs[0], refs[1:1 + k], refs[1 + k:1 + 2 * k], refs[1 + 2 * k:1 + 3 * k]
        outs = refs[1 + 3 * k:]
        for i, size in enumerate(sizes):
            g = total_ref[i:i + 1, 0:size]
            outs[i][...] = g
            outs[k + i][...], outs[2 * k + i][...], outs[3 * k + i][...] = _adamw_math(
                w_refs[i][...], g, m_refs[i][...], v_refs[i][...])

    res = pl.pallas_call(
        body, name="adamw_small", out_shape=[jax.ShapeDtypeStruct((1, size), F32) for size in sizes] * 4,
        compiler_params=_params(),
    )(total, *flat(w), *flat(m), *flat(v))
    return [{n: res[j * k + i].reshape(w[n].shape) for i, n in enumerate(SMALL)} for j in range(4)]


SHARDED = ("w_in", "w_lora_up", "a_lora_up", "w_out_a", "w_out_b", "w_out")
ROW_SHARDED = ("w_out",)
SMALL = ("norm_g", "shift_mu", "w0", "a0", "k_k", "k_a", "r_k", "lnx_w", "lnx_b", "f_bias", "q_norm_g", "k_norm_g",
         "final_norm_g")
WEIGHTS = ("norm_g", "w_in", "shift_mu", "w_lora_up", "w0", "a_lora_up", "a0", "k_k", "k_a", "r_k", "lnx_w", "lnx_b",
           "f_bias", "q_norm_g", "k_norm_g", "w_out_a", "w_out_b", "w_out", "final_norm_g")
SLAB_ROWS = 16
SLAB_COLS = SEC


def _to_slab(named, extra=None):
    rows = [jnp.pad(named[n].reshape(1, -1), ((0, 0), (0, SLAB_COLS - named[n].size))) for n in SMALL]
    if extra is not None:
        rows.append(jnp.pad(extra.reshape(1, -1), ((0, 0), (0, SLAB_COLS - extra.size))))
    rows.append(jnp.zeros((SLAB_ROWS - len(rows), SLAB_COLS), F32))
    return jnp.concatenate(rows, axis=0)


def _by_chip(g, name):
    if name in ROW_SHARDED:
        return g.reshape(N_CHIPS, g.shape[0] // N_CHIPS, g.shape[1])
    r, c = g.shape
    return g.reshape(r, N_CHIPS, c // N_CHIPS).transpose(1, 0, 2)


def _from_chips(stack, name):
    if name in ROW_SHARDED:
        return stack.reshape(-1, stack.shape[2])
    _, r, c = stack.shape
    return stack.transpose(1, 0, 2).reshape(r, N_CHIPS * c)


def kernel(x, norm_g, w_in, shift_mu, w_lora_up, w0, a_lora_up, a0, k_k, k_a, r_k, lnx_w, lnx_b, f_bias, q_norm_g, k_norm_g, w_out_a, w_out_b, w_out, final_norm_g, loss_target, m_norm_g, m_w_in, m_shift_mu, m_w_lora_up, m_w0, m_a_lora_up, m_a0, m_k_k, m_k_a, m_r_k, m_lnx_w, m_lnx_b, m_f_bias, m_q_norm_g, m_k_norm_g, m_w_out_a, m_w_out_b, m_w_out, m_final_norm_g, v_norm_g, v_w_in, v_shift_mu, v_w_lora_up, v_w0, v_a_lora_up, v_a0, v_k_k, v_k_a, v_r_k, v_lnx_w, v_lnx_b, v_f_bias, v_q_norm_g, v_k_norm_g, v_w_out_a, v_w_out_b, v_w_out, v_final_norm_g):
    w = dict(norm_g=norm_g, w_in=w_in, shift_mu=shift_mu, w_lora_up=w_lora_up, w0=w0, a_lora_up=a_lora_up, a0=a0,
             k_k=k_k, k_a=k_a, r_k=r_k, lnx_w=lnx_w, lnx_b=lnx_b, f_bias=f_bias, q_norm_g=q_norm_g,
             k_norm_g=k_norm_g, w_out_a=w_out_a, w_out_b=w_out_b, w_out=w_out, final_norm_g=final_norm_g)
    m = dict(norm_g=m_norm_g, w_in=m_w_in, shift_mu=m_shift_mu, w_lora_up=m_w_lora_up, w0=m_w0,
             a_lora_up=m_a_lora_up, a0=m_a0, k_k=m_k_k, k_a=m_k_a, r_k=m_r_k, lnx_w=m_lnx_w, lnx_b=m_lnx_b,
             f_bias=m_f_bias, q_norm_g=m_q_norm_g, k_norm_g=m_k_norm_g, w_out_a=m_w_out_a, w_out_b=m_w_out_b,
             w_out=m_w_out, final_norm_g=m_final_norm_g)
    v = dict(norm_g=v_norm_g, w_in=v_w_in, shift_mu=v_shift_mu, w_lora_up=v_w_lora_up, w0=v_w0,
             a_lora_up=v_a_lora_up, a0=v_a0, k_k=v_k_k, k_a=v_k_a, r_k=v_r_k, lnx_w=v_lnx_w, lnx_b=v_lnx_b,
             f_bias=v_f_bias, q_norm_g=v_q_norm_g, k_norm_g=v_k_norm_g, w_out_a=v_w_out_a, w_out_b=v_w_out_b,
             w_out=v_w_out, final_norm_g=v_final_norm_g)
    shapes = {n: w[n].shape for n in WEIGHTS}

    shard = {n: w[n][0].astype(BF16) for n in SHARDED}
    late = ("w_out_a", "w_out_b", "w_out")
    loras = ("w_lora_up", "a_lora_up")
    w_in_head, w_in_tail = shard["w_in"][:, :A_TAIL], shard["w_in"][:, A_TAIL:]
    shard0, shard1_head, up_stack, aup_stack = _run_on_sequencer(_gather_exchange(
        [(0, shard["w_in"]), (1, w_in_head)], [shard[n] for n in loras], split=(0,)), "gather_early", 2)
    moments = (_row_major_copy(m["w_in"][0], "m_w_in_rows"), _row_major_copy(v["w_in"][0], "v_w_in_rows"))
    shard0, moments = lax.optimization_barrier((shard0, moments))
    w_a = jnp.concatenate([shard0, shard1_head], axis=1)

    def late_weights(arrived):
        shard1_tail, shard2, shard3 = arrived[:3]
        w_b = jnp.concatenate([shard1_tail, shard2[:, :B_TAIL], jnp.zeros((D_MODEL, SEC - FOX_REAL), BF16)], axis=1)
        w_g = jnp.concatenate([shard2[:, B_TAIL:], shard3], axis=1)
        return (w_b, w_g, *[_from_chips(s, n) for n, s in zip(late, arrived[3:])])

    own = {}

    def bwd_exchange(dw_b, dw_g, dwa, dwb, dwo):
        own["tail1"] = dw_b[:, :B_HEAD]
        own["block2"] = jnp.concatenate([dw_b[:, B_HEAD:FOX_REAL], dw_g[:, :G_HEAD]], axis=1)
        own["block3"] = dw_g[:, G_HEAD:]
        own.update({n: _by_chip(g, n) for n, g in zip(late, (dwa, dwb, dwo))})
        return _scatter_exchange([(1, own["tail1"].astype(BF16)), (2, own["block2"].astype(BF16)),
                                  (3, own["block3"].astype(BF16))], [own[n].astype(BF16) for n in late])

    def tail_exchange(dw_a, dw_up, da_up):
        own["block0"], own["head1"] = dw_a[:, :SHARD_COLS], dw_a[:, SHARD_COLS:]
        own.update({n: _by_chip(g, n) for n, g in zip(loras, (dw_up, da_up))})
        return _scatter_exchange([(0, own["block0"].astype(BF16)), (1, own["head1"].astype(BF16))],
                                 [own[n].astype(BF16) for n in loras])

    small = {n: w[n] for n in SMALL}
    loss_vec, grad_x, grads, sent, sent_last = _device_grads(
        x[0], loss_target[0], small, w_a, _from_chips(up_stack, "w_lora_up"), _from_chips(aup_stack, "a_lora_up"),
        late_weights, _gather_exchange([(1, w_in_tail), (2, shard["w_in"]), (3, shard["w_in"])], [shard[n] for n in late]),
        bwd_exchange, tail_exchange)

    total = _allreduce_small(_to_slab(grads, extra=loss_vec))
    loss = (0.5 / D_MODEL) * jnp.sum(total[len(SMALL)])
    out_g, out_d, out_m, out_v = _adamw_small(total, w, m, v)

    xpos, ypos, _ = _position()
    me = (2 * xpos + ypos).astype(jnp.int32).reshape(1)
    core_sum = {n: _sum4(own[n], r, me) for n, r in zip(late, sent[3:])}
    theirs = dict(zip(late, _swap_sibling([core_sum[n] for n in late], "swap_sibling_early")))
    sent_last, out_d["norm_g"], theirs = lax.optimization_barrier((sent_last, out_d["norm_g"], theirs))
    core_sum["w_in"] = lax.switch(me[0], [
        lambda: _sum_block(own["block0"], sent_last[0]),
        lambda: jnp.concatenate([_sum_block(own["head1"], sent_last[1]), _sum_block(own["tail1"], sent[0])], axis=1),
        lambda: _sum_block(own["block2"], sent[1]),
        lambda: _sum_block(own["block3"], sent[2])])
    core_sum.update({n: _sum4(own[n], r, me) for n, r in zip(loras, sent_last[2:])})
    rest = ("w_in",) + loras
    theirs.update(zip(rest, _swap_sibling([core_sum[n] for n in rest], "swap_sibling")))
    for n in SHARDED:
        m_n, v_n = moments if n == "w_in" else (m[n][0], v[n][0])
        g, d, m2, v2 = _adamw(w[n][0], m_n, v_n, [core_sum[n], theirs[n]], "adamw_" + n)
        out_g[n], out_d[n], out_m[n], out_v[n] = (a.reshape(shapes[n]) for a in (g, d, m2, v2))

    return (loss, grad_x.reshape(x.shape), *[out_g[n] for n in WEIGHTS], *[out_d[n] for n in WEIGHTS],
            *[out_m[n] for n in WEIGHTS], *[out_v[n] for n in WEIGHTS])
```

```python
import functools
import math

import jax
import jax.numpy as jnp
from jax import lax
from jax.experimental import pallas as pl
from jax.experimental.pallas import tpu as pltpu
from jax.experimental.pallas import tpu_sc as plsc

F32 = jnp.float32
BF16 = jnp.bfloat16

D_MODEL = 1024
D_HALF = 512
HEAD = 64
N_HEADS = 8
LORA = 64
RWKV_COLS = 2176
FOX_REAL = 2056
SEC = 2176
GATE_COLS = 2048
IN_COLS = 6280
N_CHIPS = 4
SHARD_COLS = IN_COLS // N_CHIPS
A_TAIL = RWKV_COLS - SHARD_COLS
B_HEAD = SHARD_COLS - A_TAIL
B_TAIL = FOX_REAL - B_HEAD
G_HEAD = SHARD_COLS - B_TAIL
RMS_EPS = 1e-6
LNX_EPS = 64e-5
ATT_SCALE = HEAD ** -0.5
NEG = -1e30

ADAM_LR = 0.001
ADAM_B1 = 0.9
ADAM_B2 = 0.999
ADAM_EPS = 1e-08
ADAM_WD = 0.01
ADAM_STEP = 10

LANES = 128
SUBLANES = 8
VMEM_LIMIT = 56 * 1024 * 1024
MESH = pl.DeviceIdType.MESH


def _params(*sem):
    return pltpu.CompilerParams(dimension_semantics=sem if sem else None, vmem_limit_bytes=VMEM_LIMIT)


def _sigmoid(x):
    return 1.0 / (1.0 + jnp.exp(-x))


def _log_sigmoid(x):
    return jnp.minimum(x, 0.0) - jnp.log(1.0 + jnp.exp(-jnp.abs(x)))


def _head_ones():
    r = lax.broadcasted_iota(jnp.int32, (LANES, LANES), 0) >> 6
    c = lax.broadcasted_iota(jnp.int32, (LANES, LANES), 1) >> 6
    return (r == c).astype(BF16)


def _split3(x):
    hi = x.astype(BF16)
    r1 = x - hi.astype(F32)
    mid = r1.astype(BF16)
    lo = (r1 - mid.astype(F32)).astype(BF16)
    return hi, mid, lo


def _exact_dot(x, ones_bf16, ones_first=False):
    out = None
    for piece in _split3(x):
        if ones_first:
            t = jnp.dot(ones_bf16, piece, preferred_element_type=F32)
        else:
            t = jnp.dot(piece, ones_bf16, preferred_element_type=F32)
        out = t if out is None else out + t
    return out


def _head_sum(x, bd):
    n = x.shape[1] // LANES
    parts = [_exact_dot(x[:, i * LANES:(i + 1) * LANES], bd) for i in range(n)]
    return parts[0] if n == 1 else jnp.concatenate(parts, axis=1)


def _dot_nt(a, b):
    return lax.dot_general(a, b, (((1,), (1,)), ((), ())), preferred_element_type=F32)


def _dot_tn(a, b):
    return lax.dot_general(a, b, (((0,), (0,)), ((), ())), preferred_element_type=F32)


def _colsum(x):
    return jnp.sum(x, axis=0, keepdims=True)


def _rmsnorm_in(x, g, tm=512):
    s, d = x.shape

    def body(x_ref, g_ref, h_ref):
        xv = x_ref[...]
        r = lax.rsqrt(jnp.mean(xv * xv, axis=-1, keepdims=True) + RMS_EPS)
        h_ref[...] = (xv * r * g_ref[...]).astype(BF16)

    return pl.pallas_call(
        body, name="rmsnorm_in", grid=(s // tm,),
        in_specs=[pl.BlockSpec((tm, d), lambda i: (i, 0)), pl.BlockSpec((1, d), lambda i: (0, 0))],
        out_specs=pl.BlockSpec((tm, d), lambda i: (i, 0)),
        out_shape=jax.ShapeDtypeStruct((s, d), BF16), compiler_params=_params("parallel"),
    )(x, g)


def _matmul_nn(a, b, name, tm=512):
    m, k = a.shape
    n = b.shape[1]

    def body(a_ref, b_ref, o_ref):
        o_ref[...] = jnp.dot(a_ref[...], b_ref[...], preferred_element_type=F32)

    return pl.pallas_call(
        body, name=name, grid=(m // tm,),
        in_specs=[pl.BlockSpec((tm, k), lambda i: (i, 0)), pl.BlockSpec((k, n), lambda i: (0, 0))],
        out_specs=pl.BlockSpec((tm, n), lambda i: (i, 0)),
        out_shape=jax.ShapeDtypeStruct((m, n), F32), compiler_params=_params("parallel"),
    )(a, b)


def _matmul_tn_acc(at, b, name, tk=512):
    m, k = at.shape
    n = b.shape[1]

    def body(a_ref, b_ref, o_ref):
        j = pl.program_id(0)

        @pl.when(j == 0)
        def _():
            o_ref[...] = jnp.zeros_like(o_ref)

        o_ref[...] += jnp.dot(a_ref[...], b_ref[...].astype(BF16), preferred_element_type=F32)

    return pl.pallas_call(
        body, name=name, grid=(k // tk,),
        in_specs=[pl.BlockSpec((m, tk), lambda j: (0, j)), pl.BlockSpec((tk, n), lambda j: (j, 0))],
        out_specs=pl.BlockSpec((m, n), lambda j: (0, 0)),
        out_shape=jax.ShapeDtypeStruct((m, n), F32), compiler_params=_params("arbitrary"),
    )(at, b)


def _inproj_bwd(du_a, du_b, du_g, w_a, w_b, w_g, x, dx2, g, exchange=None, tm=256):
    s, d = x.shape
    nb = s // tm

    def body(*refs):
        ((da_ref, db_ref, dg_ref, wa_ref, wb_ref, wg_ref, x_ref, dx2_ref, g_ref), (gx_ref, gg_ref), _,
         moves) = _split_refs(refs, 9, 2, exchange)
        i = pl.program_id(0)
        if moves:
            moves.start(also=(i == 0))

        @pl.when(i == 0)
        def _():
            gg_ref[...] = jnp.zeros_like(gg_ref)

        dh = _dot_nt(da_ref[...].astype(BF16), wa_ref[...])
        dh += _dot_nt(db_ref[...].astype(BF16), wb_ref[...])
        dh += _dot_nt(dg_ref[...].astype(BF16), wg_ref[...])
        xv = x_ref[...]
        r = lax.rsqrt(jnp.mean(xv * xv, axis=-1, keepdims=True) + RMS_EPS)
        xh = xv * r
        gg_ref[...] += _colsum(dh * xh)
        dxh = dh * g_ref[...]
        gx_ref[...] = dx2_ref[...] + r * (dxh - xh * jnp.mean(dxh * xh, axis=-1, keepdims=True))
        if moves:
            moves.wait(also=(i == nb - 1))

    row = lambda w: pl.BlockSpec((tm, w), lambda i: (i, 0))
    full = lambda a: pl.BlockSpec(a.shape, lambda i: (0, 0))
    ex_in = exchange.operands if exchange else []
    ex_out = exchange.out_shapes if exchange else []
    res = pl.pallas_call(
        body, name="inproj_bwd", grid=(nb,),
        in_specs=[row(SEC), row(SEC), row(GATE_COLS), full(w_a), full(w_b), full(w_g), row(d), row(d), full(g)]
                 + [ANY] * len(ex_in),
        out_specs=[row(d), pl.BlockSpec((1, d), lambda i: (0, 0))] + [ANY] * len(ex_out),
        out_shape=[jax.ShapeDtypeStruct((s, d), F32), jax.ShapeDtypeStruct((1, d), F32)] + ex_out,
        scratch_shapes=exchange.scratch() if exchange else [],
        compiler_params=_params("arbitrary"),
    )(du_a, du_b, du_g, w_a, w_b, w_g, x, dx2, g, *ex_in)
    return res[0], res[1], list(res[2:])


def _rwkv_elementwise(ua, prev_row, first, mu, wl, w0, a0, kkw, kaw, bd):
    tm = ua.shape[0]
    rows = lax.broadcasted_iota(jnp.int32, (tm, 1), 0)
    prev = jnp.where(first, jnp.zeros_like(prev_row), prev_row)
    shifted = jnp.where(rows == 0, prev, pltpu.roll(ua, 1, 0))
    delta = shifted - ua
    us = ua + delta * mu
    r = us[:, 0:512]
    k0 = us[:, 512:1024]
    v = us[:, 1024:1536]
    lo = us[:, 1536:1664]
    gate = us[:, 1664:2176]
    lane = lax.broadcasted_iota(jnp.int32, (1, LANES), 1)
    th = jnp.tanh(lo)
    lin = jnp.where(lane < LORA, th, lo)
    ll = jnp.dot(lin.astype(BF16), wl, preferred_element_type=F32)
    sz = _sigmoid(w0 + ll[:, :512])
    e = sz * math.exp(-0.5)
    dec = jnp.exp(-e)
    a = _sigmoid(a0 + ll[:, 512:])
    kk0 = k0 * kkw
    ss = _head_sum(kk0 * kk0, bd)
    nrm = jnp.maximum(jnp.sqrt(ss), 1e-12)
    kk = kk0 / nrm
    k = k0 * (1.0 + (a - 1.0) * kaw)
    return dict(delta=delta, us=us, r=r, k0=k0, v=v, lo=lo, gate=gate, th=th, lin=lin, sz=sz, e=e, dec=dec,
                a=a, kk0=kk0, ss=ss, nrm=nrm, kk=kk, k=k)


def _rwkv_prep(u_a, mu, wl, w0, a0, kkw, kaw, tm=256):
    s = u_a.shape[0]

    def body(ua_ref, prev_ref, mu_ref, wl_ref, w0_ref, a0_ref, kkw_ref, kaw_ref,
             r_ref, w_ref, k_ref, v_ref, a_ref, b_ref, g_ref):
        i = pl.program_id(0)
        f = _rwkv_elementwise(ua_ref[...], prev_ref[7:8, :], i == 0, mu_ref[...], wl_ref[...], w0_ref[...],
                              a0_ref[...], kkw_ref[...], kaw_ref[...], _head_ones())
        r_ref[...] = f["r"]
        w_ref[...] = f["dec"]
        k_ref[...] = f["k"]
        v_ref[...] = f["v"]
        a_ref[...] = -f["kk"]
        b_ref[...] = f["kk"] * f["a"]
        g_ref[...] = f["gate"]

    vec = lambda w: pl.BlockSpec((1, w), lambda i: (0, 0))
    out = pl.BlockSpec((tm, D_HALF), lambda i: (i, 0))
    return pl.pallas_call(
        body, name="rwkv_prep", grid=(s // tm,),
        in_specs=[pl.BlockSpec((tm, SEC), lambda i: (i, 0)),
                  pl.BlockSpec((8, SEC), lambda i: (jnp.maximum(i * (tm // 8) - 1, 0), 0)),
                  vec(SEC), pl.BlockSpec((LANES, 2 * D_HALF), lambda i: (0, 0)),
                  vec(D_HALF), vec(D_HALF), vec(D_HALF), vec(D_HALF)],
        out_specs=[out] * 7,
        out_shape=[jax.ShapeDtypeStruct((s, D_HALF), F32)] * 7,
        compiler_params=_params("parallel"),
    )(u_a, u_a, mu, wl, w0, a0, kkw, kaw)


SCAN_TB = 128
N_PAIRS = 4


def _pair_sum(x, left):
    s_l = jnp.sum(jnp.where(left, x, 0.0), axis=1, keepdims=True)
    s_r = jnp.sum(jnp.where(left, 0.0, x), axis=1, keepdims=True)
    return jnp.where(left, s_l, s_r)


def _pair_dot(x, row_l, row_r, left):
    s_l = jnp.sum(x * row_l, axis=1, keepdims=True)
    s_r = jnp.sum(x * row_r, axis=1, keepdims=True)
    return jnp.where(left, s_l, s_r)


def _halves(rows8):
    lane = lax.broadcasted_iota(jnp.int32, rows8.shape, 1)
    keep_left = (lane & (LANES - 1)) < HEAD
    return jnp.where(keep_left, rows8, 0.0), jnp.where(keep_left, 0.0, rows8)


def _quad_consts():
    lane = lax.broadcasted_iota(jnp.int32, (HEAD, 2 * LANES), 1)
    rowi = lax.broadcasted_iota(jnp.int32, (HEAD, 2 * LANES), 0)
    diag2 = rowi == (lane & (HEAD - 1))
    r = lax.broadcasted_iota(jnp.int32, (2 * LANES, 2 * LANES), 0) >> 6
    c = lax.broadcasted_iota(jnp.int32, (2 * LANES, 2 * LANES), 1) >> 6
    return diag2, (r == c).astype(BF16)


def _rows_to_columns(x8, diag2, bd2):
    lhs = jnp.concatenate([jnp.where(diag2, x8[i:i + 1], 0.0).astype(BF16) for i in range(SUBLANES)], axis=0)
    return jnp.dot(lhs, bd2, preferred_element_type=F32)


def _diag_rows(qtile, diag2, bd2, sub_row2):
    res = jnp.dot(qtile, bd2, preferred_element_type=F32)
    out = jnp.zeros((SUBLANES, 2 * LANES), F32)
    for i in range(SUBLANES):
        out = jnp.where(sub_row2 == i, _colsum(jnp.where(diag2, res[i * HEAD:(i + 1) * HEAD], 0.0)), out)
    return out


def _store_tile(qbuf, slot, p, i, x):
    qbuf[slot, p // 2, i * HEAD:(i + 1) * HEAD, (p % 2) * LANES:(p % 2 + 1) * LANES] = x.astype(BF16)


def _left_half():
    return lax.broadcasted_iota(jnp.int32, (HEAD, LANES), 1) < HEAD


def _split_refs(refs, n_rows, n_out, exchange):
    n_in = len(exchange.operands) if exchange else 0
    n_ex_out = len(exchange.out_shapes) if exchange else 0
    refs = list(refs)
    rows, refs = refs[:n_rows], refs[n_rows:]
    ex_in, refs = refs[:n_in], refs[n_in:]
    outs, refs = refs[:n_out], refs[n_out:]
    ex_out, refs = refs[:n_ex_out], refs[n_ex_out:]
    scratch, sems = (refs[:-3], refs[-3:]) if exchange else (refs, None)
    moves = exchange.moves(ex_in, ex_out, sems) if exchange else None
    return rows, outs, scratch, moves


def _wkv_fwd(r, w, k, a, b, v, exchange=None):
    s = r.shape[0]
    tb = SCAN_TB
    nb = s // tb

    def body(*refs):
        (r_ref, w_ref, k_ref, a_ref, b_ref, v_ref), (y_ref, st_ref), (state, vbuf, qbuf), moves = _split_refs(
            refs, 6, 2, exchange)
        g = pl.program_id(0)
        if moves:
            moves.start(also=(g == 0))

        @pl.when(g == 0)
        def _():
            state[...] = jnp.zeros_like(state)
            qbuf[...] = jnp.zeros_like(qbuf)

        left = _left_half()
        diag2, bd2 = _quad_consts()
        sub_row2 = lax.broadcasted_iota(jnp.int32, (SUBLANES, 2 * LANES), 0)
        groups = tb // SUBLANES
        quads = [slice(g2 * 2 * LANES, (g2 + 1) * 2 * LANES) for g2 in range(2)]

        def rows_of(q):
            return pl.ds(pl.multiple_of(q * SUBLANES, SUBLANES), SUBLANES)

        def v_tiles(q, slot):
            v8 = v_ref[rows_of(q), :]
            for g2 in range(2):
                vbuf[slot, g2] = _rows_to_columns(v8[:, quads[g2]], diag2, bd2)

        def chain(q, slot):
            rows8 = rows_of(q)
            a8, w8, b8, k8, r8 = (x[rows8, :] for x in (a_ref, w_ref, b_ref, k_ref, r_ref))
            pairs = [slice(p * LANES, (p + 1) * LANES) for p in range(N_PAIRS)]
            a_next = pltpu.roll(a8, SUBLANES - 1, 0)
            (a8_l, a8_r), (wa8_l, wa8_r) = _halves(a8), _halves(w8 * a_next)
            ba8 =jnp.concatenate([_pair_sum(b8[:, pr] * a_next[:, pr], left[0:SUBLANES]) for pr in pairs], axis=1)
            ka8 = jnp.concatenate([_pair_sum(k8[:, pr] * a_next[:, pr], left[0:SUBLANES]) for pr in pairs], axis=1)
            sp = [state[p] for p in range(N_PAIRS)]
            for i in range(0, SUBLANES, 2):
                r0, r1 = slice(i, i + 1), slice(i + 1, i + 2)
                sums = [(_pair_dot(sp[p], a8_l[r0, pairs[p]], a8_r[r0, pairs[p]], left),
                         _pair_dot(sp[p], wa8_l[r0, pairs[p]], wa8_r[r0, pairs[p]], left)) for p in range(N_PAIRS)]
                sa0, sa1 = [s[0] for s in sums], [s[1] for s in sums]
                for p in range(N_PAIRS):
                    pr = pairs[p]
                    inner = slice((p % 2) * LANES, (p % 2 + 1) * LANES)
                    vt0 = vbuf[slot, p // 2, i * HEAD:(i + 1) * HEAD, inner]
                    vt1 = vbuf[slot, p // 2, (i + 1) * HEAD:(i + 2) * HEAD, inner]
                    sa_next = sa1[p] + sa0[p] * ba8[r0, pr] + vt0 * ka8[r0, pr]
                    s1 = sp[p] * w8[r0, pr] + sa0[p] * b8[r0, pr] + vt0 * k8[r0, pr]
                    st_ref[q * SUBLANES + i, p] = s1
                    _store_tile(qbuf, slot, p, i, s1 * r8[r0, pr])
                    s2 = s1 * w8[r1, pr] + sa_next * b8[r1, pr] + vt1 * k8[r1, pr]
                    st_ref[q * SUBLANES + i + 1, p] = s2
                    _store_tile(qbuf, slot, p, i + 1, s2 * r8[r1, pr])
                    sp[p] = s2
            for p in range(N_PAIRS):
                state[p] = sp[p]

        def y_rows(q, slot):
            for g2 in range(2):
                y_ref[rows_of(q), quads[g2]] = _diag_rows(qbuf[slot, g2], diag2, bd2, sub_row2)

        v_tiles(0, 0)

        def two_groups(j, carry):
            q0 = 2 * j
            v_tiles(q0 + 1, 1)
            chain(q0, 0)
            y_rows(jnp.maximum(q0 - 1, 0), 1)
            v_tiles(jnp.minimum(q0 + 2, groups - 1), 0)
            chain(q0 + 1, 1)
            y_rows(q0, 0)
            return carry

        lax.fori_loop(0, groups // 2, two_groups, 0)
        y_rows(groups - 1, 1)
        if moves:
            moves.wait(also=(g == nb - 1))

    rows = pl.BlockSpec((tb, D_HALF), lambda g: (g, 0))
    ex_in = exchange.operands if exchange else []
    ex_out = exchange.out_shapes if exchange else []
    res = pl.pallas_call(
        body, name="wkv_fwd", grid=(nb,),
        in_specs=[rows] * 6 + [ANY] * len(ex_in),
        out_specs=[rows, pl.BlockSpec((tb, N_PAIRS, HEAD, LANES), lambda g: (g, 0, 0, 0))] + [ANY] * len(ex_out),
        out_shape=[jax.ShapeDtypeStruct((s, D_HALF), F32),
                   jax.ShapeDtypeStruct((s, N_PAIRS, HEAD, LANES), F32)] + ex_out,
        scratch_shapes=[pltpu.VMEM((N_PAIRS, HEAD, LANES), F32),
                        pltpu.VMEM((2, 2, SUBLANES * HEAD, 2 * LANES), F32),
                        pltpu.VMEM((2, 2, SUBLANES * HEAD, 2 * LANES), BF16)]
                       + (exchange.scratch() if exchange else []),
        compiler_params=_params("arbitrary"),
    )(r, w, k, a, b, v, *ex_in)
    return res[0], res[1], list(res[2:])


def _wkv_bwd(r, w, k, a, b, v, dy, st, exchange=None):
    s = r.shape[0]
    tb = SCAN_TB
    nb = s // tb

    def body(*refs):
        ((r_ref, w_ref, k_ref, a_ref, b_ref, v_ref, dy_ref, st_ref, before_ref),
         (dr_ref, dw_ref, dk_ref, dv_ref, da_ref, db_ref), (dstate, vbuf, qbuf, sbuf),
         moves) = _split_refs(refs, 9, 6, exchange)
        g = pl.program_id(0)
        first_block = g == nb - 1
        if moves:
            moves.start(also=(g == 0))

        @pl.when(g == 0)
        def _():
            dstate[...] = jnp.zeros_like(dstate)
            qbuf[...] = jnp.zeros_like(qbuf)

        left = _left_half()
        diag2, bd2 = _quad_consts()
        sub_row = lax.broadcasted_iota(jnp.int32, (SUBLANES, LANES), 0)
        sub_row2 = lax.broadcasted_iota(jnp.int32, (SUBLANES, 2 * LANES), 0)
        groups = tb // SUBLANES
        quads = [slice(g2 * 2 * LANES, (g2 + 1) * 2 * LANES) for g2 in range(2)]
        row_refs = (dr_ref, dw_ref, dk_ref, da_ref, db_ref)

        def rows_of(q):
            return pl.ds(pl.multiple_of(q * SUBLANES, SUBLANES), SUBLANES)

        def state_before(q, i, p):
            if i > 0:
                return st_ref[q * SUBLANES + i - 1, p]
            return jnp.where(q == 0, jnp.where(first_block, 0.0, before_ref[0, p]),
                             st_ref[jnp.maximum(q * SUBLANES - 1, 0), p])

        def column_tiles(q, slot):
            rows8 = rows_of(q)
            for kind, ref in enumerate((v_ref, dy_ref)):
                x8 = ref[rows8, :]
                for g2 in range(2):
                    vbuf[slot, kind, g2] = _rows_to_columns(x8[:, quads[g2]], diag2, bd2)
            a8 = a_ref[rows8, :]
            for i in range(SUBLANES):
                for p in range(N_PAIRS):
                    _store_tile(sbuf, 0, p, i, state_before(q, i, p) * a8[i:i + 1, p * LANES:(p + 1) * LANES])
            for g2 in range(2):
                vbuf[slot, 2, g2] = jnp.dot(sbuf[0, g2], bd2, preferred_element_type=F32)

        def chain(q, slot):
            rows8 = rows_of(q)
            a8, w8, b8, k8, r8 = (x[rows8, :] for x in (a_ref, w_ref, b_ref, k_ref, r_ref))
            b8_l, b8_r = _halves(b8)
            dsp = [dstate[p] for p in range(N_PAIRS)]
            outs = [[jnp.zeros((SUBLANES, LANES), F32) for _ in row_refs] for _ in range(N_PAIRS)]
            after = [st_ref[q * SUBLANES + SUBLANES - 1, p] for p in range(N_PAIRS)]
            for i in reversed(range(SUBLANES)):
                row = slice(i, i + 1)
                pl_ = [slice(p * LANES, (p + 1) * LANES) for p in range(N_PAIRS)]
                tile = [(p // 2, slice(i * HEAD, (i + 1) * HEAD), slice((p % 2) * LANES, (p % 2 + 1) * LANES))
                        for p in range(N_PAIRS)]
                sp = [state_before(q, i, p) for p in range(N_PAIRS)]
                dyt = [vbuf[(slot, 1) + tile[p]] for p in range(N_PAIRS)]
                ds = [dsp[p] + dyt[p] * r8[row, pl_[p]] for p in range(N_PAIRS)]
                dsa = [_pair_dot(ds[p], b8_l[row, pl_[p]], b8_r[row, pl_[p]], left) for p in range(N_PAIRS)]
                sa = [vbuf[(slot, 2) + tile[p]] for p in range(N_PAIRS)]
                for p in range(N_PAIRS):
                    ar, wr, br, kr = (x[row, pl_[p]] for x in (a8, w8, b8, k8))
                    vt = vbuf[(slot, 0) + tile[p]]
                    dsp[p] = ds[p] * wr + dsa[p] * ar
                    new = (_colsum(after[p] * dyt[p]), _colsum(ds[p] * sp[p]), _colsum(ds[p] * vt),
                           _colsum(sp[p] * dsa[p]), _colsum(ds[p] * sa[p]))
                    outs[p] = [jnp.where(sub_row == i, n, o) for n, o in zip(new, outs[p])]
                    _store_tile(qbuf, slot, p, i, ds[p] * kr)
                after = sp
            for p in range(N_PAIRS):
                dstate[p] = dsp[p]
                for ref, o in zip(row_refs, outs[p]):
                    ref[rows8, p * LANES:(p + 1) * LANES] = o

        def dv_rows(q, slot):
            for g2 in range(2):
                dv_ref[rows_of(q), quads[g2]] = _diag_rows(qbuf[slot, g2], diag2, bd2, sub_row2)

        column_tiles(groups - 1, 0)

        def two_groups(j, carry):
            q0 = groups - 1 - 2 * j
            column_tiles(q0 - 1, 1)
            chain(q0, 0)
            dv_rows(jnp.minimum(q0 + 1, groups - 1), 1)
            column_tiles(jnp.maximum(q0 - 2, 0), 0)
            chain(q0 - 1, 1)
            dv_rows(q0, 0)
            return carry

        lax.fori_loop(0, groups // 2, two_groups, 0)
        dv_rows(0, 1)
        if moves:
            moves.wait(also=(g == nb - 1))

    rows = pl.BlockSpec((tb, D_HALF), lambda g: (nb - 1 - g, 0))
    ex_in = exchange.operands if exchange else []
    ex_out = exchange.out_shapes if exchange else []
    res = pl.pallas_call(
        body, name="wkv_bwd", grid=(nb,),
        in_specs=[rows] * 7 + [pl.BlockSpec((tb, N_PAIRS, HEAD, LANES), lambda g: (nb - 1 - g, 0, 0, 0)),
                               pl.BlockSpec((1, N_PAIRS, HEAD, LANES),
                                            lambda g: (jnp.maximum((nb - 1 - g) * tb - 1, 0), 0, 0, 0))]
                 + [ANY] * len(ex_in),
        out_specs=[rows] * 6 + [ANY] * len(ex_out),
        out_shape=[jax.ShapeDtypeStruct((s, D_HALF), F32)] * 6 + ex_out,
        scratch_shapes=[pltpu.VMEM((N_PAIRS, HEAD, LANES), F32),
                        pltpu.VMEM((2, 3, 2, SUBLANES * HEAD, 2 * LANES), F32),
                        pltpu.VMEM((2, 2, SUBLANES * HEAD, 2 * LANES), BF16),
                        pltpu.VMEM((1, 2, SUBLANES * HEAD, 2 * LANES), BF16)]
                       + (exchange.scratch() if exchange else []),
        compiler_params=_params("arbitrary"),
    )(r, w, k, a, b, v, dy, st, st, *ex_in)
    return list(res[:6]), list(res[6:])


def _rwkv_post_math(y, r, k, v, gate, lw, lb, rk, bd):
    mean = _head_sum(y, bd) * (1.0 / HEAD)
    yc = y - mean
    var = _head_sum(yc * yc, bd) * (1.0 / HEAD)
    rstd = lax.rsqrt(var + LNX_EPS)
    yn = yc * rstd
    rkk = _head_sum(r * k * rk, bd)
    sg = _sigmoid(gate)
    pre = yn * lw + lb + rkk * v
    return yn, rstd, rkk, sg, pre


def _rwkv_post(y, r, k, v, gate, lw, lb, rk, tm=256):
    s = y.shape[0]

    def body(y_ref, r_ref, k_ref, v_ref, g_ref, lw_ref, lb_ref, rk_ref, o_ref):
        gate_v = g_ref[...]
        _, _, _, sg, pre = _rwkv_post_math(y_ref[...], r_ref[...], k_ref[...], v_ref[...], gate_v,
                                           lw_ref[...], lb_ref[...], rk_ref[...], _head_ones())
        o_ref[...] = pre * (gate_v * sg)

    blk = pl.BlockSpec((tm, D_HALF), lambda i: (i, 0))
    vec = pl.BlockSpec((1, D_HALF), lambda i: (0, 0))
    return pl.pallas_call(
        body, name="rwkv_post", grid=(s // tm,),
        in_specs=[blk] * 5 + [vec] * 3, out_specs=blk,
        out_shape=jax.ShapeDtypeStruct((s, D_HALF), F32), compiler_params=_params("parallel"),
    )(y, r, k, v, gate, lw, lb, rk)


def _rwkv_post_bwd(dmix, y, r, k, v, gate, lw, lb, rk, tm=256):
    s = y.shape[0]

    def body(dm_ref, y_ref, r_ref, k_ref, v_ref, g_ref, lw_ref, lb_ref, rk_ref,
             dy_ref, dr_ref, dk_ref, dv_ref, dg_ref, dlw_ref, dlb_ref, drk_ref):
        i = pl.program_id(0)

        @pl.when(i == 0)
        def _():
            dlw_ref[...] = jnp.zeros_like(dlw_ref)
            dlb_ref[...] = jnp.zeros_like(dlb_ref)
            drk_ref[...] = jnp.zeros_like(drk_ref)

        bd = _head_ones()
        rv, kv, vv, gate_v, lw_v, rk_v = r_ref[...], k_ref[...], v_ref[...], g_ref[...], lw_ref[...], rk_ref[...]
        yn, rstd, rkk, sg, pre = _rwkv_post_math(y_ref[...], rv, kv, vv, gate_v, lw_v, lb_ref[...], rk_v, bd)
        dm = dm_ref[...]
        dg_ref[...] = dm * pre * (sg * (1.0 + gate_v * (1.0 - sg)))
        dpre = dm * (gate_v * sg)
        dlw_ref[...] += _colsum(dpre * yn)
        dlb_ref[...] += _colsum(dpre)
        dyn = dpre * lw_v
        m1 = _head_sum(dyn, bd) * (1.0 / HEAD)
        m2 = _head_sum(dyn * yn, bd) * (1.0 / HEAD)
        dy_ref[...] = rstd * (dyn - m1 - yn * m2)
        dv_ref[...] = dpre * rkk
        drkk = _head_sum(dpre * vv, bd)
        dr_ref[...] = drkk * kv * rk_v
        dk_ref[...] = drkk * rv * rk_v
        drk_ref[...] += _colsum(drkk * rv * kv)

    blk = pl.BlockSpec((tm, D_HALF), lambda i: (i, 0))
    vec = pl.BlockSpec((1, D_HALF), lambda i: (0, 0))
    return pl.pallas_call(
        body, name="rwkv_post_bwd", grid=(s // tm,),
        in_specs=[blk] * 6 + [vec] * 3, out_specs=[blk] * 5 + [vec] * 3,
        out_shape=[jax.ShapeDtypeStruct((s, D_HALF), F32)] * 5 + [jax.ShapeDtypeStruct((1, D_HALF), F32)] * 3,
        compiler_params=_params("arbitrary"),
    )(dmix, y, r, k, v, gate, lw, lb, rk)


def _rwkv_prep_bwd(u_a, grads, mu, wl, w0, a0, kkw, kaw, tm=256):
    s = u_a.shape[0]
    nb = s // tm

    def body(ua_ref, prev_ref, drs_ref, dws_ref, dks_ref, dvs_ref, das_ref, dbs_ref, drb_ref, dkb_ref, dvb_ref,
             dgt_ref, mu_ref, wl_ref, w0_ref, a0_ref, kkw_ref, kaw_ref,
             du_ref, dmu_ref, dwl_ref, dw0_ref, da0_ref, dkkw_ref, dkaw_ref, carry):
        i = pl.program_id(0)

        @pl.when(i == 0)
        def _():
            carry[...] = jnp.zeros_like(carry)
            for ref in (dmu_ref, dwl_ref, dw0_ref, da0_ref, dkkw_ref, dkaw_ref):
                ref[...] = jnp.zeros_like(ref)

        bd = _head_ones()
        mu_v, wl_v, kkw_v, kaw_v = mu_ref[...], wl_ref[...], kkw_ref[...], kaw_ref[...]
        f = _rwkv_elementwise(ua_ref[...], prev_ref[7:8, :], i == nb - 1, mu_v, wl_v, w0_ref[...],
                              a0_ref[...], kkw_v, kaw_v, bd)
        a, kk, k0 = f["a"], f["kk"], f["k0"]
        dk = dks_ref[...] + dkb_ref[...]
        dbs = dbs_ref[...]
        dkk = dbs * a - das_ref[...]
        da = dbs * kk + dk * k0 * kaw_v
        dk0 = dk * (1.0 + (a - 1.0) * kaw_v)
        dkaw_ref[...] += _colsum(dk * k0 * (a - 1.0))
        inv = 1.0 / f["nrm"]
        proj = _head_sum(dkk * kk, bd)
        dkk0 = jnp.where(f["ss"] > 1e-24, (dkk - kk * proj) * inv, dkk * inv)
        dk0 = dk0 + dkk0 * kkw_v
        dkkw_ref[...] += _colsum(dkk0 * k0)
        dza = da * a * (1.0 - a)
        da0_ref[...] += _colsum(dza)
        dz = -dws_ref[...] * f["dec"] * f["e"] * (1.0 - f["sz"])
        dw0_ref[...] += _colsum(dz)
        dll = jnp.concatenate([dz, dza], axis=1).astype(BF16)
        dwl_ref[...] += _dot_tn(f["lin"].astype(BF16), dll)
        dlin = _dot_nt(dll, wl_v)
        lane = lax.broadcasted_iota(jnp.int32, (1, LANES), 1)
        th = f["th"]
        dlo = jnp.where(lane < LORA, dlin * (1.0 - th * th), dlin)
        dus = jnp.concatenate([drs_ref[...] + drb_ref[...], dk0, dvs_ref[...] + dvb_ref[...], dlo, dgt_ref[...]],
                              axis=1)
        dmu_ref[...] += _colsum(dus * f["delta"])
        g1 = dus * mu_v
        rows = lax.broadcasted_iota(jnp.int32, (tm, 1), 0)
        up = jnp.where(rows == tm - 1, carry[...], pltpu.roll(g1, tm - 1, 0))
        du_ref[...] = dus - g1 + up
        carry[...] = g1[0:1, :]

    rev = lambda w: pl.BlockSpec((tm, w), lambda i: (nb - 1 - i, 0))
    vec = lambda w: pl.BlockSpec((1, w), lambda i: (0, 0))
    wl_spec = pl.BlockSpec((LANES, 2 * D_HALF), lambda i: (0, 0))
    return pl.pallas_call(
        body, name="rwkv_prep_bwd", grid=(nb,),
        in_specs=[rev(SEC), pl.BlockSpec((8, SEC), lambda i: (jnp.maximum((nb - 1 - i) * (tm // 8) - 1, 0), 0))]
                 + [rev(D_HALF)] * 10 + [vec(SEC), wl_spec] + [vec(D_HALF)] * 4,
        out_specs=[rev(SEC), vec(SEC), wl_spec] + [vec(D_HALF)] * 4,
        out_shape=[jax.ShapeDtypeStruct((s, SEC), F32), jax.ShapeDtypeStruct((1, SEC), F32),
                   jax.ShapeDtypeStruct((LANES, 2 * D_HALF), F32)] + [jax.ShapeDtypeStruct((1, D_HALF), F32)] * 4,
        scratch_shapes=[pltpu.VMEM((1, SEC), F32)],
        compiler_params=_params("arbitrary"),
    )(u_a, u_a, *grads, mu, wl, w0, a0, kkw, kaw)


def _tri(tm, lower):
    r = lax.broadcasted_iota(jnp.int32, (tm, tm), 0)
    c = lax.broadcasted_iota(jnp.int32, (tm, tm), 1)
    return ((r >= c) if lower else (r <= c)).astype(BF16)


def _head_rms(x, g, bd):
    rinv = lax.rsqrt(_head_sum(x * x, bd) * (1.0 / HEAD) + RMS_EPS)
    xh = x * rinv
    return xh, rinv, xh * g


def _fox_prep(u_b, fb, qg, kg, tm=256):
    s = u_b.shape[0]

    def body(ub_ref, fb_ref, qg_ref, kg_ref, q_ref, k_ref, v_ref, cc_ref, cr_ref, carry):
        i = pl.program_id(0)

        @pl.when(i == 0)
        def _():
            carry[...] = jnp.zeros_like(carry)

        bd = _head_ones()
        _, _, qn = _head_rms(ub_ref[:, 0:512], qg_ref[...], bd)
        _, _, kn = _head_rms(ub_ref[:, 512:1024], kg_ref[...], bd)
        q_ref[...] = (qn * ATT_SCALE).astype(BF16)
        k_ref[...] = kn.astype(BF16)
        v_ref[...] = ub_ref[:, 1024:1536].astype(BF16)
        lane = lax.broadcasted_iota(jnp.int32, (1, LANES), 1)
        logf = jnp.where(lane < N_HEADS, _log_sigmoid(ub_ref[:, 2048:2176] + fb_ref[...]), 0.0)
        cum = _exact_dot(logf, _tri(tm, True), ones_first=True) + carry[...]
        for h in range(N_HEADS):
            cc_ref[h] = jnp.broadcast_to(cum[:, h:h + 1], (tm, LANES))
        cr_ref[...] = jnp.transpose(cum)[0:N_HEADS, :]
        carry[...] = cum[tm - 1:tm, :]

    blk = pl.BlockSpec((tm, D_HALF), lambda i: (i, 0))
    return pl.pallas_call(
        body, name="fox_prep", grid=(s // tm,),
        in_specs=[pl.BlockSpec((tm, SEC), lambda i: (i, 0)), pl.BlockSpec((1, LANES), lambda i: (0, 0)),
                  pl.BlockSpec((1, D_HALF), lambda i: (0, 0)), pl.BlockSpec((1, D_HALF), lambda i: (0, 0))],
        out_specs=[blk, blk, blk, pl.BlockSpec((N_HEADS, tm, LANES), lambda i: (0, i, 0)),
                   pl.BlockSpec((N_HEADS, tm), lambda i: (0, i))],
        out_shape=[jax.ShapeDtypeStruct((s, D_HALF), BF16)] * 3
                  + [jax.ShapeDtypeStruct((N_HEADS, s, LANES), F32), jax.ShapeDtypeStruct((N_HEADS, s), F32)],
        scratch_shapes=[pltpu.VMEM((1, LANES), F32)],
        compiler_params=_params("arbitrary"),
    )(u_b, fb, qg, kg)


ATT_T = 256


def _attn_fwd(q, k, v, cc, cr, u_b):
    s = q.shape[0]
    t = ATT_T
    nblk = s // t

    def body(q_ref, k_ref, v_ref, cc_ref, cr_ref, g_ref, o_ref, mix_ref, lse_ref, m_sc, l_sc, acc_sc):
        i = pl.program_id(0)
        j = pl.program_id(1)

        @pl.when(j == 0)
        def _():
            m_sc[...] = jnp.full_like(m_sc, NEG)
            l_sc[...] = jnp.zeros_like(l_sc)
            acc_sc[...] = jnp.zeros_like(acc_sc)

        def tile(on_diagonal):
            causal = _causal_tile(t) if on_diagonal else None
            left = lax.broadcasted_iota(jnp.int32, (1, LANES), 1) < HEAD
            for p in range(N_PAIRS):
                lanes = slice(p * LANES, (p + 1) * LANES)
                q2, k2, v2 = q_ref[:, lanes], k_ref[:, lanes], v_ref[:, lanes]
                acc2 = acc_sc[:, lanes]
                for e in range(2):
                    h = 2 * p + e
                    msk = left if e == 0 else jnp.logical_not(left)
                    sc = _dot_nt(jnp.where(msk, q2, jnp.zeros_like(q2)), k2)
                    sc = sc + (_wide(cc_ref[h]) - cr_ref[h:h + 1, :])
                    if on_diagonal:
                        sc = jnp.where(causal, sc, NEG)
                    m_prev = m_sc[h]
                    m_new = jnp.maximum(m_prev, jnp.max(sc, axis=1, keepdims=True))
                    alpha = jnp.exp(m_prev - m_new)
                    pm = jnp.exp(sc - _wide(m_new))
                    l_sc[h] = alpha * l_sc[h] + jnp.sum(pm, axis=1, keepdims=True)
                    m_sc[h] = m_new
                    pv = jnp.dot(pm.astype(BF16), v2, preferred_element_type=F32)
                    acc2 = jnp.where(msk, alpha * acc2 + pv, acc2)
                acc_sc[:, lanes] = acc2

        pl.when(j < i)(functools.partial(tile, False))
        pl.when(j == i)(functools.partial(tile, True))

        @pl.when(j == i)
        def _():
            left = lax.broadcasted_iota(jnp.int32, (1, LANES), 1) < HEAD
            for p in range(N_PAIRS):
                lanes = slice(p * LANES, (p + 1) * LANES)
                inv = jnp.where(left, 1.0 / l_sc[2 * p], 1.0 / l_sc[2 * p + 1])
                o = acc_sc[:, lanes] * inv
                o_ref[:, lanes] = o
                gate = g_ref[:, lanes]
                mix_ref[:, lanes] = o * (gate * _sigmoid(gate))
            for h in range(N_HEADS):
                lse_ref[h] = m_sc[h] + jnp.log(l_sc[h])

    qblk = pl.BlockSpec((t, D_HALF), lambda i, j: (i, 0))
    kblk = pl.BlockSpec((t, D_HALF), lambda i, j: (jnp.minimum(i, j), 0))
    return pl.pallas_call(
        body, name="fox_attn_fwd", grid=(nblk, nblk),
        in_specs=[qblk, kblk, kblk, pl.BlockSpec((N_HEADS, t, LANES), lambda i, j: (0, i, 0)),
                  pl.BlockSpec((N_HEADS, t), lambda i, j: (0, jnp.minimum(i, j))),
                  pl.BlockSpec((t, D_HALF), lambda i, j: (i, 3))],
        out_specs=[qblk, qblk, pl.BlockSpec((N_HEADS, t, LANES), lambda i, j: (0, i, 0))],
        out_shape=[jax.ShapeDtypeStruct((s, D_HALF), F32), jax.ShapeDtypeStruct((s, D_HALF), F32),
                   jax.ShapeDtypeStruct((N_HEADS, s, LANES), F32)],
        scratch_shapes=[pltpu.VMEM((N_HEADS, t, LANES), F32), pltpu.VMEM((N_HEADS, t, LANES), F32),
                        pltpu.VMEM((t, D_HALF), F32)],
        compiler_params=_params("parallel", "arbitrary"),
    )(q, k, v, cc, cr, u_b)


def _fox_post_bwd(dmix, o, u_b, tm=256):
    s = o.shape[0]

    def body(dm_ref, o_ref, g_ref, do_ref, dg_ref):
        gate = g_ref[...]
        sg = _sigmoid(gate)
        dm = dm_ref[...]
        do_ref[...] = (dm * (gate * sg)).astype(BF16)
        dg_ref[...] = dm * o_ref[...] * (sg * (1.0 + gate * (1.0 - sg)))

    blk = pl.BlockSpec((tm, D_HALF), lambda i: (i, 0))
    return pl.pallas_call(
        body, name="fox_post_bwd", grid=(s // tm,),
        in_specs=[blk, blk, pl.BlockSpec((tm, D_HALF), lambda i: (i, 3))], out_specs=[blk] * 2,
        out_shape=[jax.ShapeDtypeStruct((s, D_HALF), BF16), jax.ShapeDtypeStruct((s, D_HALF), F32)],
        compiler_params=_params("parallel"),
    )(dmix, o, u_b)


def _causal_tile(t):
    return lax.broadcasted_iota(jnp.int32, (t, t), 0) >= lax.broadcasted_iota(jnp.int32, (t, t), 1)


def _wide(x):
    return jnp.concatenate([x, x], axis=1)


def _attn_probs(q2, k2, v2, do2, msk, causal, bias, lse_rows):
    zero = jnp.zeros_like(q2)
    qh = jnp.where(msk, q2, zero)
    doh = jnp.where(msk, do2, zero)
    sc = _dot_nt(qh, k2) + bias
    if causal is not None:
        sc = jnp.where(causal, sc, NEG)
    pm = jnp.exp(sc - _wide(lse_rows))
    dp = _dot_nt(doh, v2)
    return qh, doh, pm, dp


def _attn_bwd_rowdot(q, k, v, do, lse, cc, cr):
    s = q.shape[0]
    t = ATT_T
    nblk = s // t

    def body(q_ref, k_ref, v_ref, do_ref, lse_ref, cc_ref, cr_ref, dd_ref, acc):
        i = pl.program_id(0)
        j = pl.program_id(1)

        @pl.when(j == 0)
        def _():
            acc[...] = jnp.zeros_like(acc)

        def tile(on_diagonal):
            causal = _causal_tile(t) if on_diagonal else None
            left = lax.broadcasted_iota(jnp.int32, (1, LANES), 1) < HEAD
            for p in range(N_PAIRS):
                lanes = slice(p * LANES, (p + 1) * LANES)
                q2, k2, v2, do2 = q_ref[:, lanes], k_ref[:, lanes], v_ref[:, lanes], do_ref[:, lanes]
                for e in range(2):
                    h = 2 * p + e
                    msk = left if e == 0 else jnp.logical_not(left)
                    bias = _wide(cc_ref[h]) - cr_ref[h:h + 1, :]
                    _, _, pm, dp = _attn_probs(q2, k2, v2, do2, msk, causal, bias, lse_ref[h])
                    acc[h] += jnp.sum(pm * dp, axis=1, keepdims=True)

        pl.when(j < i)(functools.partial(tile, False))
        pl.when(j == i)(functools.partial(tile, True))

        @pl.when(j == i)
        def _():
            dd_ref[...] = acc[...]

    qblk = pl.BlockSpec((t, D_HALF), lambda i, j: (i, 0))
    qcol = pl.BlockSpec((N_HEADS, t, LANES), lambda i, j: (0, i, 0))
    kblk = pl.BlockSpec((t, D_HALF), lambda i, j: (jnp.minimum(i, j), 0))
    return pl.pallas_call(
        body, name="fox_attn_rowdot", grid=(nblk, nblk),
        in_specs=[qblk, kblk, kblk, qblk, qcol, qcol, pl.BlockSpec((N_HEADS, t), lambda i, j: (0, jnp.minimum(i, j)))],
        out_specs=qcol, out_shape=jax.ShapeDtypeStruct((N_HEADS, s, LANES), F32),
        scratch_shapes=[pltpu.VMEM((N_HEADS, t, LANES), F32)],
        compiler_params=_params("parallel", "arbitrary"),
    )(q, k, v, do, lse, cc, cr)


def _attn_bwd(q, k, v, do, lse, dd, cc, cr):
    s = q.shape[0]
    t = ATT_T
    nblk = s // t

    def body(q_ref, k_ref, v_ref, do_ref, lse_ref, dd_ref, cc_ref, cr_ref,
             dq_ref, dk_ref, dv_ref, dcr_ref, dk_sc, dv_sc, dcr_sc):
        j = pl.program_id(0)
        i = pl.program_id(1)

        @pl.when(jnp.logical_and(j == 0, i == 0))
        def _():
            dq_ref[...] = jnp.zeros_like(dq_ref)

        @pl.when(i == 0)
        def _():
            dk_sc[...] = jnp.zeros_like(dk_sc)
            dv_sc[...] = jnp.zeros_like(dv_sc)
            dcr_sc[...] = jnp.zeros_like(dcr_sc)

        def tile(on_diagonal):
            causal = _causal_tile(t) if on_diagonal else None
            left = lax.broadcasted_iota(jnp.int32, (1, LANES), 1) < HEAD
            qrows = pl.ds(pl.multiple_of(i * t, t), t)
            for p in range(N_PAIRS):
                lanes = slice(p * LANES, (p + 1) * LANES)
                q2, k2, v2, do2 = q_ref[:, lanes], k_ref[:, lanes], v_ref[:, lanes], do_ref[:, lanes]
                zero = jnp.zeros_like(q2)
                dq2 = jnp.zeros((t, LANES), F32)
                dk2 = jnp.zeros((t, LANES), F32)
                dv2 = jnp.zeros((t, LANES), F32)
                for e in range(2):
                    h = 2 * p + e
                    msk = left if e == 0 else jnp.logical_not(left)
                    bias = _wide(cc_ref[h]) - cr_ref[h:h + 1, :]
                    qh, doh, pm, dp = _attn_probs(q2, k2, v2, do2, msk, causal, bias, lse_ref[h])
                    dsc = pm * (dp - _wide(dd_ref[h]))
                    dsb = dsc.astype(BF16)
                    dv2 += _dot_tn(pm.astype(BF16), doh)
                    dk2 += _dot_tn(dsb, qh)
                    dq2 += jnp.dot(dsb, jnp.where(msk, k2, zero), preferred_element_type=F32)
                    dcr_sc[h:h + 1, :] += -_colsum(dsc)
                dq_ref[qrows, lanes] += dq2 * ATT_SCALE
                dk_sc[:, lanes] += dk2
                dv_sc[:, lanes] += dv2

        pl.when(i > j)(functools.partial(tile, False))
        pl.when(i == j)(functools.partial(tile, True))

        @pl.when(i == nblk - 1)
        def _():
            dk_ref[...] = dk_sc[...]
            dv_ref[...] = dv_sc[...]
            dcr_ref[...] = dcr_sc[...]

    qblk = pl.BlockSpec((t, D_HALF), lambda j, i: (jnp.maximum(i, j), 0))
    qcol = pl.BlockSpec((N_HEADS, t, LANES), lambda j, i: (0, jnp.maximum(i, j), 0))
    kblk = pl.BlockSpec((t, D_HALF), lambda j, i: (j, 0))
    return pl.pallas_call(
        body, name="fox_attn_bwd", grid=(nblk, nblk),
        in_specs=[qblk, kblk, kblk, qblk, qcol, qcol, qcol, pl.BlockSpec((N_HEADS, t), lambda j, i: (0, j))],
        out_specs=[pl.BlockSpec((s, D_HALF), lambda j, i: (0, 0)), kblk, kblk,
                   pl.BlockSpec((N_HEADS, t), lambda j, i: (0, j))],
        out_shape=[jax.ShapeDtypeStruct((s, D_HALF), F32)] * 3 + [jax.ShapeDtypeStruct((N_HEADS, s), F32)],
        scratch_shapes=[pltpu.VMEM((t, D_HALF), F32), pltpu.VMEM((t, D_HALF), F32), pltpu.VMEM((N_HEADS, t), F32)],
        compiler_params=_params("arbitrary", "arbitrary"),
    )(q, k, v, do, lse, dd, cc, cr)


def _fox_prep_bwd(u_b, dq, dk, dv, dgate, dcum, fb, qg, kg, tm=256):
    s = u_b.shape[0]
    nb = s // tm

    def body(ub_ref, dq_ref, dk_ref, dv_ref, dg_ref, dc_ref, fb_ref, qg_ref, kg_ref,
             du_ref, dqg_ref, dkg_ref, dfb_ref, carry):
        i = pl.program_id(0)

        @pl.when(i == 0)
        def _():
            carry[...] = jnp.zeros_like(carry)
            dqg_ref[...] = jnp.zeros_like(dqg_ref)
            dkg_ref[...] = jnp.zeros_like(dkg_ref)
            dfb_ref[...] = jnp.zeros_like(dfb_ref)

        bd = _head_ones()
        for lo, g_ref, d_ref, dgain_ref in ((0, qg_ref, dq_ref, dqg_ref), (512, kg_ref, dk_ref, dkg_ref)):
            gain = g_ref[...]
            xh, rinv, _ = _head_rms(ub_ref[:, lo:lo + 512], gain, bd)
            dn = d_ref[...]
            dgain_ref[...] += _colsum(dn * xh)
            dxh = dn * gain
            du_ref[:, lo:lo + 512] = rinv * (dxh - xh * (_head_sum(dxh * xh, bd) * (1.0 / HEAD)))
        du_ref[:, 1024:1536] = dv_ref[...]
        du_ref[:, 1536:2048] = dg_ref[...]
        lane = lax.broadcasted_iota(jnp.int32, (1, LANES), 1)
        dc = dc_ref[...]
        dlogf = _exact_dot(dc, _tri(tm, False), ones_first=True) + carry[...]
        carry[...] += _colsum(dc)
        fl = ub_ref[:, 2048:2176] + fb_ref[...]
        dfl = jnp.where(lane < N_HEADS, dlogf * (1.0 - _sigmoid(fl)), 0.0)
        du_ref[:, 2048:2176] = dfl
        dfb_ref[...] += _colsum(dfl)

    rev = lambda w: pl.BlockSpec((tm, w), lambda i: (nb - 1 - i, 0))
    vec = lambda w: pl.BlockSpec((1, w), lambda i: (0, 0))
    return pl.pallas_call(
        body, name="fox_prep_bwd", grid=(nb,),
        in_specs=[rev(SEC)] + [rev(D_HALF)] * 4 + [rev(LANES), vec(LANES), vec(D_HALF), vec(D_HALF)],
        out_specs=[rev(SEC), vec(D_HALF), vec(D_HALF), vec(LANES)],
        out_shape=[jax.ShapeDtypeStruct((s, SEC), F32), jax.ShapeDtypeStruct((1, D_HALF), F32),
                   jax.ShapeDtypeStruct((1, D_HALF), F32), jax.ShapeDtypeStruct((1, LANES), F32)],
        scratch_shapes=[pltpu.VMEM((1, LANES), F32)],
        compiler_params=_params("arbitrary"),
    )(u_b, dq, dk, dv, dgate, dcum, fb, qg, kg)


def _merge(mix_a, mix_b, u_g, x, tgt, wa, wb, wo, fg, tm=256):
    s, d = x.shape

    def body(ma_ref, mb_ref, ug_ref, x_ref, t_ref, wa_ref, wb_ref, wo_ref, fg_ref,
             dx2_ref, dma_ref, dmb_ref, dug_ref, dwa_ref, dwb_ref, dwo_ref, dfg_ref, loss_ref):
        i = pl.program_id(0)

        @pl.when(i == 0)
        def _():
            for ref in (dwa_ref, dwb_ref, dwo_ref, dfg_ref, loss_ref):
                ref[...] = jnp.zeros_like(ref)

        wa_v, wb_v, wo_v, fg_v = wa_ref[...], wb_ref[...], wo_ref[...], fg_ref[...]
        ma = ma_ref[...].astype(BF16)
        mb = mb_ref[...].astype(BF16)
        ya = jnp.dot(ma, wa_v, preferred_element_type=F32)
        yb = jnp.dot(mb, wb_v, preferred_element_type=F32)
        sa = _sigmoid(ug_ref[:, 0:d])
        sb = _sigmoid(ug_ref[:, d:2 * d])
        merged = (sa * ya + sb * yb).astype(BF16)
        x2 = x_ref[...] + jnp.dot(merged, wo_v, preferred_element_type=F32)
        r2 = lax.rsqrt(jnp.mean(x2 * x2, axis=-1, keepdims=True) + RMS_EPS)
        x2h = x2 * r2
        err = x2h * fg_v - t_ref[...]
        loss_ref[...] += _colsum(err * err)
        dy = err * (1.0 / d)
        dfg_ref[...] += _colsum(dy * x2h)
        dx2h = dy * fg_v
        dx2 = r2 * (dx2h - x2h * jnp.mean(dx2h * x2h, axis=-1, keepdims=True))
        dx2_ref[...] = dx2
        dx2b = dx2.astype(BF16)
        dmerged = _dot_nt(dx2b, wo_v)
        dwo_ref[...] += _dot_tn(merged, dx2b)
        dya = dmerged * sa
        dyb = dmerged * sb
        dug_ref[:, 0:d] = dya * ya * (1.0 - sa)
        dug_ref[:, d:2 * d] = dyb * yb * (1.0 - sb)
        dyab = dya.astype(BF16)
        dybb = dyb.astype(BF16)
        dma_ref[...] = _dot_nt(dyab, wa_v)
        dmb_ref[...] = _dot_nt(dybb, wb_v)
        dwa_ref[...] += _dot_tn(ma, dyab)
        dwb_ref[...] += _dot_tn(mb, dybb)

    row = lambda w: pl.BlockSpec((tm, w), lambda i: (i, 0))
    full = lambda a: pl.BlockSpec(a.shape, lambda i: (0, 0))
    fshape = lambda a: jax.ShapeDtypeStruct(a.shape, F32)
    return pl.pallas_call(
        body, name="merge_fwd_bwd", grid=(s // tm,),
        in_specs=[row(D_HALF), row(D_HALF), row(GATE_COLS), row(d), row(d), full(wa), full(wb), full(wo), full(fg)],
        out_specs=[row(d), row(D_HALF), row(D_HALF), row(GATE_COLS), full(wa), full(wb), full(wo), full(fg), full(fg)],
        out_shape=[jax.ShapeDtypeStruct((s, d), F32), jax.ShapeDtypeStruct((s, D_HALF), F32),
                   jax.ShapeDtypeStruct((s, D_HALF), F32), jax.ShapeDtypeStruct((s, GATE_COLS), F32),
                   fshape(wa), fshape(wb), fshape(wo), fshape(fg), fshape(fg)],
        compiler_params=_params("arbitrary"),
    )(mix_a, mix_b, u_g, x, tgt, wa, wb, wo, fg)


def _lora_weight(w_up, a_up):
    z = jnp.zeros((LORA, D_HALF), w_up.dtype)
    return jnp.concatenate([jnp.concatenate([w_up, z], axis=1), jnp.concatenate([z, a_up], axis=1)], axis=0)


def _device_grads(x, tgt, p, w_a, w_up, a_up, late_weights, fwd_exchange=None, bwd_exchange=None, tail_exchange=None):
    wl = _lora_weight(w_up, a_up)
    rk = p["r_k"].reshape(1, D_HALF)
    fb = jnp.pad(p["f_bias"], ((0, 0), (0, LANES - N_HEADS)))
    qg = jnp.tile(p["q_norm_g"], (1, N_HEADS))
    kg = jnp.tile(p["k_norm_g"], (1, N_HEADS))
    fg = p["final_norm_g"].reshape(1, D_MODEL)
    mixer = (p["shift_mu"], wl, p["w0"], p["a0"], p["k_k"], p["k_a"])

    h = _rmsnorm_in(x, p["norm_g"])
    u_a = _matmul_nn(h, w_a, "inproj_rwkv")
    r, dec, k, v, av, bv, gate_a = _rwkv_prep(u_a, *mixer)
    y, st, arrived = _wkv_fwd(r, dec, k, av, bv, v, fwd_exchange)
    mix_a = _rwkv_post(y, r, k, v, gate_a, p["lnx_w"], p["lnx_b"], rk)

    w_b, w_g, w_out_a, w_out_b, w_out = late_weights(arrived)
    u_b = _matmul_nn(h, w_b, "inproj_fox")
    u_g = _matmul_nn(h, w_g, "inproj_gate")
    q, kn, vb, cc, cr = _fox_prep(u_b, fb, qg, kg)
    o, mix_b, lse = _attn_fwd(q, kn, vb, cc, cr, u_b)

    dx2, dmix_a, dmix_b, du_g, dwa, dwb, dwo, dfg, loss_vec = _merge(
        mix_a, mix_b, u_g, x, tgt, w_out_a, w_out_b, w_out, fg)

    do, dgate_b = _fox_post_bwd(dmix_b, o, u_b)
    dd = _attn_bwd_rowdot(q, kn, vb, do, lse, cc, cr)
    dq, dk_att, dv_att, dcr = _attn_bwd(q, kn, vb, do, lse, dd, cc, cr)
    dcum = jnp.pad(dcr.T, ((0, 0), (0, LANES - N_HEADS)))
    du_b, dqg, dkg, dfb = _fox_prep_bwd(u_b, dq, dk_att, dv_att, dgate_b, dcum, fb, qg, kg)
    h_t = h.T
    dw_b = _matmul_tn_acc(h_t, du_b, "dw_fox")
    dw_g = _matmul_tn_acc(h_t, du_g, "dw_gate")

    dy, dr_b, dk_b, dv_b, dgate_a, dlw, dlb, drk = _rwkv_post_bwd(
        dmix_a, y, r, k, v, gate_a, p["lnx_w"], p["lnx_b"], rk)
    scan_grads, sent = _wkv_bwd(r, dec, k, av, bv, v, dy, st,
                                bwd_exchange(dw_b, dw_g, dwa, dwb, dwo) if bwd_exchange else None)
    du_a, dmu, dwl, dw0, da0, dkkw, dkaw = _rwkv_prep_bwd(u_a, (*scan_grads, dr_b, dk_b, dv_b, dgate_a), *mixer)
    dw_a = _matmul_tn_acc(h_t, du_a, "dw_rwkv")
    dw_up, da_up = dwl[:LORA, :D_HALF], dwl[LORA:, D_HALF:]
    sent_last = _run_on_sequencer(tail_exchange(dw_a, dw_up, da_up), "scatter_tail", 1) if tail_exchange else []
    grad_x, dnorm_g, _ = _inproj_bwd(du_a, du_b, du_g, w_a, w_b, w_g, x, dx2, p["norm_g"])

    grads = dict(
        norm_g=dnorm_g, w_in=(dw_a, dw_b, dw_g), shift_mu=dmu,
        w_lora_up=dw_up, w0=dw0, a_lora_up=da_up, a0=da0, k_k=dkkw, k_a=dkaw,
        r_k=drk.reshape(1, N_HEADS, HEAD), lnx_w=dlw, lnx_b=dlb, f_bias=dfb[:, :N_HEADS],
        q_norm_g=dqg.reshape(N_HEADS, HEAD).sum(axis=0, keepdims=True),
        k_norm_g=dkg.reshape(N_HEADS, HEAD).sum(axis=0, keepdims=True),
        w_out_a=dwa, w_out_b=dwb, w_out=dwo, final_norm_g=dfg.reshape(D_MODEL))
    return loss_vec, grad_x, grads, sent, sent_last


CHIP_FLIPS = ((1, 0), (0, 1), (1, 1))
ANY = pl.BlockSpec(memory_space=pl.ANY)


def _position():
    return lax.axis_index("x"), lax.axis_index("y"), lax.axis_index("c")


def _flip(v, f):
    return 1 - v if f else v


def _both(a, b):
    if a is None:
        return b
    return a if b is None else jnp.logical_and(a, b)


def _when(cond, fn):
    if cond is None:
        fn()
    else:
        pl.when(cond)(fn)


class _Moves:
    def __init__(self, send_sems, recv_sems, local_sems):
        self.send_sems, self.recv_sems, self.local_sems = send_sems, recv_sems, local_sems
        self.remote, self.local = [], []

    def send(self, src, dst, peer, landing, send_if=None, recv_if=None):
        k = len(self.remote)
        sems = dict(send_sem=self.send_sems.at[k], recv_sem=self.recv_sems.at[k], device_id=peer, device_id_type=MESH)
        out = pltpu.make_async_remote_copy(src_ref=src, dst_ref=dst, **sems)
        arrival = pltpu.make_async_remote_copy(src_ref=src, dst_ref=landing, **sems)
        self.remote.append((out, arrival, send_if, recv_if))

    def copy(self, src, dst, cond=None):
        cp = pltpu.make_async_copy(src, dst, self.local_sems.at[len(self.local)])
        self.local.append((cp, cond))

    def start(self, also=None):
        for cp, cond in self.local:
            _when(_both(also, cond), cp.start)
        for out, _, send_if, _ in self.remote:
            _when(_both(also, send_if), out.start)

    def wait_arrivals(self, also=None):
        for _, arrival, _, recv_if in self.remote:
            _when(_both(also, recv_if), arrival.wait_recv)

    def wait_sent(self, also=None):
        for out, _, send_if, _ in self.remote:
            _when(_both(also, send_if), out.wait_send)
        for cp, cond in self.local:
            _when(_both(also, cond), cp.wait)

    def wait(self, also=None):
        self.wait_arrivals(also)
        self.wait_sent(also)


class _Exchange:
    def __init__(self, operands, out_shapes, n_remote, n_local, build, n_relay=0, relay=None):
        self.operands, self.out_shapes = list(operands), list(out_shapes)
        self.n_remote, self.n_local, self.build = n_remote, n_local, build
        self.n_relay, self.relay = n_relay, relay

    def scratch(self):
        return [pltpu.SemaphoreType.DMA((self.n_remote,)), pltpu.SemaphoreType.DMA((self.n_remote,)),
                pltpu.SemaphoreType.DMA((max(self.n_local, 1),))]

    def moves(self, in_refs, out_refs, sems):
        mv = _Moves(*sems)
        self.build(mv, in_refs, out_refs)
        return mv

    def run_alone(self, name):
        n_in, n_out = len(self.operands), len(self.out_shapes)
        relay_scratch = [pltpu.SemaphoreType.DMA((self.n_relay,))] * 2 if self.relay else []

        def body(*refs):
            ins, outs, sems = refs[:n_in], refs[n_in:n_in + n_out], refs[n_in + n_out:]
            mv = self.moves(ins, outs, sems[:3])
            mv.start()
            mv.wait_arrivals()
            if self.relay:
                passed = _Moves(sems[3], sems[4], None)
                self.relay(passed, ins, outs)
                passed.start()
                passed.wait()
            mv.wait_sent()

        return pl.pallas_call(
            body, name=name, in_specs=[ANY] * n_in, out_specs=[ANY] * n_out, out_shape=self.out_shapes,
            scratch_shapes=self.scratch() + relay_scratch, compiler_params=pltpu.CompilerParams(has_side_effects=True),
        )(*self.operands)


def _run_on_sequencer(exchange, name, collective_id):
    ins = [jax.new_ref(a, memory_space=pltpu.MemorySpace.HBM) for a in exchange.operands]
    outs = [jax.empty_ref(s, memory_space=pltpu.MemorySpace.HBM) for s in exchange.out_shapes]
    relay_scratch = [pltpu.SemaphoreType.DMA((exchange.n_relay,))] * 2 if exchange.relay else []

    def launch(*sems):
        x, y, c = _position()
        peers = [(_flip(x, fx), _flip(y, fy), c) for fx, fy in CHIP_FLIPS] + ([(x, y, 1 - c)] if exchange.relay else [])
        barrier = pltpu.get_barrier_semaphore()
        for peer in peers:
            pl.semaphore_signal(barrier, inc=1, device_id=peer, device_id_type=MESH)
        pl.semaphore_wait(barrier, len(peers))
        moves = exchange.moves(ins, outs, sems[:3])
        moves.start()
        moves.wait_arrivals()
        if exchange.relay:
            passed = _Moves(sems[3], sems[4], None)
            exchange.relay(passed, ins, outs)
            passed.start()
            passed.wait()
        moves.wait_sent()

    pl.kernel(launch, mesh=plsc.ScalarSubcoreMesh(axis_name="sequencer", num_cores=1), name=name,
              scratch_types=tuple(exchange.scratch() + relay_scratch),
              compiler_params=pltpu.CompilerParams(collective_id=collective_id))()
    return [o[...] for o in outs]


def _row_major_copy(a, name):
    r, c = a.shape
    tr = _row_tile(r)

    def body(a_ref, o_ref):
        o_ref[...] = a_ref[...]

    blk = pl.BlockSpec((tr, c), lambda i: (i, 0))
    return pl.pallas_call(body, name=name, grid=(r // tr,), in_specs=[blk], out_specs=blk,
                          out_shape=jax.ShapeDtypeStruct(a.shape, a.dtype), compiler_params=_params("parallel"))(a)


def _is_chip(x, y, chip):
    return jnp.logical_and(x == chip // 2, y == chip % 2)


def _gather_exchange(from_chip, from_all, split=()):
    n1, n2 = len(from_chip), len(from_all)

    def rows_of(t, c):
        half = from_chip[t][1].shape[0] // 2
        return pl.ds(c * half, half)

    def build(mv, ins, outs):
        x, y, c = _position()
        me = 2 * x + y
        for t, (chip, _) in enumerate(from_chip):
            mv.copy(ins[t], outs[t], cond=_is_chip(x, y, chip))
        for t in range(n2):
            mv.copy(ins[n1 + t], outs[n1 + t].at[me])
        for fx, fy in CHIP_FLIPS:
            px, py = _flip(x, fx), _flip(y, fy)
            peer = (px, py, c)
            for t, (chip, _) in enumerate(from_chip):
                part = rows_of(t, c) if t in split else slice(None)
                mv.send(ins[t].at[part], outs[t].at[part], peer, landing=outs[t].at[part],
                        send_if=_is_chip(x, y, chip), recv_if=_is_chip(px, py, chip))
            for t in range(n2):
                mv.send(ins[n1 + t], outs[n1 + t].at[me], peer, landing=outs[n1 + t].at[2 * px + py])

    def relay(mv, ins, outs):
        x, y, c = _position()
        for t in split:
            came = jnp.logical_not(_is_chip(x, y, from_chip[t][0]))
            mv.send(outs[t].at[rows_of(t, c)], outs[t].at[rows_of(t, c)], (x, y, 1 - c),
                    landing=outs[t].at[rows_of(t, 1 - c)], send_if=came, recv_if=came)

    arrays = [a for _, a in from_chip] + list(from_all)
    shapes = [jax.ShapeDtypeStruct(a.shape, a.dtype) for _, a in from_chip]
    shapes += [jax.ShapeDtypeStruct((N_CHIPS,) + a.shape, a.dtype) for a in from_all]
    return _Exchange(arrays, shapes, len(CHIP_FLIPS) * (n1 + n2), n1 + n2, build,
                     n_relay=len(split), relay=relay if split else None)


def _scatter_exchange(to_chip, to_all):
    n1, n2 = len(to_chip), len(to_all)

    def build(mv, ins, outs):
        x, y, c = _position()
        for f, (fx, fy) in enumerate(CHIP_FLIPS):
            px, py = _flip(x, fx), _flip(y, fy)
            peer = (px, py, c)
            for t, (chip, _) in enumerate(to_chip):
                mv.send(ins[t], outs[t].at[f], peer, landing=outs[t].at[f],
                        send_if=_is_chip(px, py, chip), recv_if=_is_chip(x, y, chip))
            for t in range(n2):
                mv.send(ins[n1 + t].at[2 * px + py], outs[n1 + t].at[f], peer, landing=outs[n1 + t].at[f])

    arrays = [a for _, a in to_chip] + list(to_all)
    shapes = [jax.ShapeDtypeStruct((len(CHIP_FLIPS),) + a.shape, a.dtype) for _, a in to_chip]
    shapes += [jax.ShapeDtypeStruct((len(CHIP_FLIPS),) + a.shape[1:], a.dtype) for a in to_all]
    return _Exchange(arrays, shapes, len(CHIP_FLIPS) * (n1 + n2), 0, build)


def _swap_sibling(tensors, name):
    n = len(tensors)

    def body(*refs):
        ins, outs = refs[:n], refs[n:2 * n]
        send_sems, recv_sems = refs[2 * n:]
        x, y, c = _position()
        copies = [pltpu.make_async_remote_copy(
            src_ref=ins[t], dst_ref=outs[t], send_sem=send_sems.at[t], recv_sem=recv_sems.at[t],
            device_id=(x, y, 1 - c), device_id_type=MESH) for t in range(n)]
        for cp in copies:
            cp.start()
        for cp in copies:
            cp.wait_recv()
        for cp in copies:
            cp.wait_send()

    return pl.pallas_call(
        body, name=name, in_specs=[ANY] * n, out_specs=[ANY] * n,
        out_shape=[jax.ShapeDtypeStruct(a.shape, a.dtype) for a in tensors],
        scratch_shapes=[pltpu.SemaphoreType.DMA((n,)), pltpu.SemaphoreType.DMA((n,))],
        compiler_params=pltpu.CompilerParams(has_side_effects=True),
    )(*tensors)


def _allreduce_small(slab):
    n_dev = 8
    flips = [(fx, fy, fc) for fx in (0, 1) for fy in (0, 1) for fc in (0, 1)][1:]

    def body(x_ref, o_ref, buf, send_sems, recv_sems):
        x, y, c = _position()
        me = 4 * x + 2 * y + c
        buf[me] = x_ref[...]
        copies = []
        for k, (fx, fy, fc) in enumerate(flips):
            px, py, pc = _flip(x, fx), _flip(y, fy), _flip(c, fc)
            sems = dict(send_sem=send_sems.at[k], recv_sem=recv_sems.at[k], device_id=(px, py, pc), device_id_type=MESH)
            out = pltpu.make_async_remote_copy(src_ref=x_ref, dst_ref=buf.at[me], **sems)
            arrival = pltpu.make_async_remote_copy(src_ref=x_ref, dst_ref=buf.at[4 * px + 2 * py + pc], **sems)
            out.start()
            copies.append((out, arrival))
        for _, arrival in copies:
            arrival.wait_recv()
        for out, _ in copies:
            out.wait_send()
        total = buf[0]
        for d in range(1, n_dev):
            total = total + buf[d]
        o_ref[...] = total

    return pl.pallas_call(
        body, name="allreduce_small",
        in_specs=[pl.BlockSpec(memory_space=pltpu.VMEM)], out_specs=pl.BlockSpec(memory_space=pltpu.VMEM),
        out_shape=jax.ShapeDtypeStruct(slab.shape, slab.dtype),
        scratch_shapes=[pltpu.VMEM((n_dev,) + slab.shape, slab.dtype),
                        pltpu.SemaphoreType.DMA((len(flips),)), pltpu.SemaphoreType.DMA((len(flips),))],
        compiler_params=pltpu.CompilerParams(has_side_effects=True),
    )(slab)


def _row_tile(r):
    return min(r, 256)


def _sum4(stack, recv, me):
    _, r, c = stack.shape
    tr = _row_tile(r)

    def body(me_ref, own_ref, recv_ref, o_ref):
        o_ref[...] = (((own_ref[...] + recv_ref[0].astype(F32)) + recv_ref[1].astype(F32))
                      + recv_ref[2].astype(F32))

    return pl.pallas_call(
        body, name="sum_partials",
        grid_spec=pltpu.PrefetchScalarGridSpec(
            num_scalar_prefetch=1, grid=(r // tr,),
            in_specs=[pl.BlockSpec((None, tr, c), lambda i, me_ref: (me_ref[0], i, 0)),
                      pl.BlockSpec((len(CHIP_FLIPS), tr, c), lambda i, me_ref: (0, i, 0))],
            out_specs=pl.BlockSpec((tr, c), lambda i, me_ref: (i, 0))),
        out_shape=jax.ShapeDtypeStruct((r, c), F32), compiler_params=_params("parallel"),
    )(me, stack, recv)


def _sum_block(own, recv):
    r, c = own.shape
    tr = _row_tile(r)

    def body(own_ref, recv_ref, o_ref):
        o_ref[...] = (((own_ref[...] + recv_ref[0].astype(F32)) + recv_ref[1].astype(F32))
                      + recv_ref[2].astype(F32))

    return pl.pallas_call(
        body, name="sum_block", grid=(r // tr,),
        in_specs=[pl.BlockSpec((tr, c), lambda i: (i, 0)), pl.BlockSpec((len(CHIP_FLIPS), tr, c), lambda i: (0, i, 0))],
        out_specs=pl.BlockSpec((tr, c), lambda i: (i, 0)),
        out_shape=jax.ShapeDtypeStruct((r, c), F32), compiler_params=_params("parallel"),
    )(own, recv)


def _adamw_math(w, g, m, v):
    m = ADAM_B1 * m + (1.0 - ADAM_B1) * g
    v = ADAM_B2 * v + (1.0 - ADAM_B2) * (g * g)
    m_hat = m / (1.0 - ADAM_B1 ** ADAM_STEP)
    v_hat = v / (1.0 - ADAM_B2 ** ADAM_STEP)
    delta = -ADAM_LR * (m_hat / (jnp.sqrt(v_hat) + ADAM_EPS) + ADAM_WD * w)
    return delta, m, v


def _adamw(w, m, v, g_parts, name):
    r, c = w.shape
    tr = _row_tile(r)
    n = len(g_parts)

    def body(*refs):
        w_ref, m_ref, v_ref = refs[:3]
        g_refs = refs[3:3 + n]
        g_out, d_out, m_out, v_out = refs[3 + n:]
        g = g_refs[0][...]
        for ref in g_refs[1:]:
            g = g + ref[...]
        g_out[...] = g
        d_out[...], m_out[...], v_out[...] = _adamw_math(w_ref[...], g, m_ref[...], v_ref[...])

    blk = pl.BlockSpec((tr, c), lambda i: (i, 0))
    return pl.pallas_call(
        body, name=name, grid=(r // tr,), in_specs=[blk] * (3 + n), out_specs=[blk] * 4,
        out_shape=[jax.ShapeDtypeStruct((r, c), F32)] * 4, compiler_params=_params("parallel"),
    )(w, m, v, *g_parts)


def _adamw_small(total, w, m, v):
    sizes = [w[n].size for n in SMALL]
    flat = lambda d: [d[n].reshape(1, -1) for n in SMALL]
    k = len(SMALL)

    def body(*refs):
        total_ref, w_refs, m_refs, v_refs = refs[0], refs[1:1 + k], refs[1 + k:1 + 2 * k], refs[1 + 2 * k:1 + 3 * k]
        outs = refs[1 + 3 * k:]
        for i, size in enumerate(sizes):
            g = total_ref[i:i + 1, 0:size]
            outs[i][...] = g
            outs[k + i][...], outs[2 * k + i][...], outs[3 * k + i][...] = _adamw_math(
                w_refs[i][...], g, m_refs[i][...], v_refs[i][...])

    res = pl.pallas_call(
        body, name="adamw_small", out_shape=[jax.ShapeDtypeStruct((1, size), F32) for size in sizes] * 4,
        compiler_params=_params(),
    )(total, *flat(w), *flat(m), *flat(v))
    return [{n: res[j * k + i].reshape(w[n].shape) for i, n in enumerate(SMALL)} for j in range(4)]


SHARDED = ("w_in", "w_lora_up", "a_lora_up", "w_out_a", "w_out_b", "w_out")
ROW_SHARDED = ("w_out",)
SMALL = ("norm_g", "shift_mu", "w0", "a0", "k_k", "k_a", "r_k", "lnx_w", "lnx_b", "f_bias", "q_norm_g", "k_norm_g",
         "final_norm_g")
WEIGHTS = ("norm_g", "w_in", "shift_mu", "w_lora_up", "w0", "a_lora_up", "a0", "k_k", "k_a", "r_k", "lnx_w", "lnx_b",
           "f_bias", "q_norm_g", "k_norm_g", "w_out_a", "w_out_b", "w_out", "final_norm_g")
SLAB_ROWS = 16
SLAB_COLS = SEC


def _to_slab(named, extra=None):
    rows = [jnp.pad(named[n].reshape(1, -1), ((0, 0), (0, SLAB_COLS - named[n].size))) for n in SMALL]
    if extra is not None:
        rows.append(jnp.pad(extra.reshape(1, -1), ((0, 0), (0, SLAB_COLS - extra.size))))
    rows.append(jnp.zeros((SLAB_ROWS - len(rows), SLAB_COLS), F32))
    return jnp.concatenate(rows, axis=0)


def _by_chip(g, name):
    if name in ROW_SHARDED:
        return g.reshape(N_CHIPS, g.shape[0] // N_CHIPS, g.shape[1])
    r, c = g.shape
    return g.reshape(r, N_CHIPS, c // N_CHIPS).transpose(1, 0, 2)


def _from_chips(stack, name):
    if name in ROW_SHARDED:
        return stack.reshape(-1, stack.shape[2])
    _, r, c = stack.shape
    return stack.transpose(1, 0, 2).reshape(r, N_CHIPS * c)


def kernel(x, norm_g, w_in, shift_mu, w_lora_up, w0, a_lora_up, a0, k_k, k_a, r_k, lnx_w, lnx_b, f_bias, q_norm_g, k_norm_g, w_out_a, w_out_b, w_out, final_norm_g, loss_target, m_norm_g, m_w_in, m_shift_mu, m_w_lora_up, m_w0, m_a_lora_up, m_a0, m_k_k, m_k_a, m_r_k, m_lnx_w, m_lnx_b, m_f_bias, m_q_norm_g, m_k_norm_g, m_w_out_a, m_w_out_b, m_w_out, m_final_norm_g, v_norm_g, v_w_in, v_shift_mu, v_w_lora_up, v_w0, v_a_lora_up, v_a0, v_k_k, v_k_a, v_r_k, v_lnx_w, v_lnx_b, v_f_bias, v_q_norm_g, v_k_norm_g, v_w_out_a, v_w_out_b, v_w_out, v_final_norm_g):
    w = dict(norm_g=norm_g, w_in=w_in, shift_mu=shift_mu, w_lora_up=w_lora_up, w0=w0, a_lora_up=a_lora_up, a0=a0,
             k_k=k_k, k_a=k_a, r_k=r_k, lnx_w=lnx_w, lnx_b=lnx_b, f_bias=f_bias, q_norm_g=q_norm_g,
             k_norm_g=k_norm_g, w_out_a=w_out_a, w_out_b=w_out_b, w_out=w_out, final_norm_g=final_norm_g)
    m = dict(norm_g=m_norm_g, w_in=m_w_in, shift_mu=m_shift_mu, w_lora_up=m_w_lora_up, w0=m_w0,
             a_lora_up=m_a_lora_up, a0=m_a0, k_k=m_k_k, k_a=m_k_a, r_k=m_r_k, lnx_w=m_lnx_w, lnx_b=m_lnx_b,
             f_bias=m_f_bias, q_norm_g=m_q_norm_g, k_norm_g=m_k_norm_g, w_out_a=m_w_out_a, w_out_b=m_w_out_b,
             w_out=m_w_out, final_norm_g=m_final_norm_g)
    v = dict(norm_g=v_norm_g, w_in=v_w_in, shift_mu=v_shift_mu, w_lora_up=v_w_lora_up, w0=v_w0,
             a_lora_up=v_a_lora_up, a0=v_a0, k_k=v_k_k, k_a=v_k_a, r_k=v_r_k, lnx_w=v_lnx_w, lnx_b=v_lnx_b,
             f_bias=v_f_bias, q_norm_g=v_q_norm_g, k_norm_g=v_k_norm_g, w_out_a=v_w_out_a, w_out_b=v_w_out_b,
             w_out=v_w_out, final_norm_g=v_final_norm_g)
    shapes = {n: w[n].shape for n in WEIGHTS}

    shard = {n: w[n][0].astype(BF16) for n in SHARDED}
    late = ("w_out_a", "w_out_b", "w_out")
    loras = ("w_lora_up", "a_lora_up")
    w_in_head, w_in_tail = shard["w_in"][:, :A_TAIL], shard["w_in"][:, A_TAIL:]
    shard0, shard1_head, up_stack, aup_stack = _run_on_sequencer(_gather_exchange(
        [(0, shard["w_in"]), (1, w_in_head)], [shard[n] for n in loras], split=(0,)), "gather_early", 2)
    moments = (_row_major_copy(m["w_in"][0], "m_w_in_rows"), _row_major_copy(v["w_in"][0], "v_w_in_rows"))
    shard0, moments = lax.optimization_barrier((shard0, moments))
    w_a = jnp.concatenate([shard0, shard1_head], axis=1)

    def late_weights(arrived):
        shard1_tail, shard2, shard3 = arrived[:3]
        w_b = jnp.concatenate([shard1_tail, shard2[:, :B_TAIL], jnp.zeros((D_MODEL, SEC - FOX_REAL), BF16)], axis=1)
        w_g = jnp.concatenate([shard2[:, B_TAIL:], shard3], axis=1)
        return (w_b, w_g, *[_from_chips(s, n) for n, s in zip(late, arrived[3:])])

    own = {}

    def bwd_exchange(dw_b, dw_g, dwa, dwb, dwo):
        own["tail1"] = dw_b[:, :B_HEAD]
        own["block2"] = jnp.concatenate([dw_b[:, B_HEAD:FOX_REAL], dw_g[:, :G_HEAD]], axis=1)
        own["block3"] = dw_g[:, G_HEAD:]
        own.update({n: _by_chip(g, n) for n, g in zip(late, (dwa, dwb, dwo))})
        return _scatter_exchange([(1, own["tail1"].astype(BF16)), (2, own["block2"].astype(BF16)),
                                  (3, own["block3"].astype(BF16))], [own[n].astype(BF16) for n in late])

    def tail_exchange(dw_a, dw_up, da_up):
        own["block0"], own["head1"] = dw_a[:, :SHARD_COLS], dw_a[:, SHARD_COLS:]
        own.update({n: _by_chip(g, n) for n, g in zip(loras, (dw_up, da_up))})
        return _scatter_exchange([(0, own["block0"].astype(BF16)), (1, own["head1"].astype(BF16))],
                                 [own[n].astype(BF16) for n in loras])

    small = {n: w[n] for n in SMALL}
    loss_vec, grad_x, grads, sent, sent_last = _device_grads(
        x[0], loss_target[0], small, w_a, _from_chips(up_stack, "w_lora_up"), _from_chips(aup_stack, "a_lora_up"),
        late_weights, _gather_exchange([(1, w_in_tail), (2, shard["w_in"]), (3, shard["w_in"])], [shard[n] for n in late]),
        bwd_exchange, tail_exchange)

    total = _allreduce_small(_to_slab(grads, extra=loss_vec))
    loss = (0.5 / D_MODEL) * jnp.sum(total[len(SMALL)])
    out_g, out_d, out_m, out_v = _adamw_small(total, w, m, v)

    xpos, ypos, _ = _position()
    me = (2 * xpos + ypos).astype(jnp.int32).reshape(1)
    core_sum = {n: _sum4(own[n], r, me) for n, r in zip(late, sent[3:])}
    theirs = dict(zip(late, _swap_sibling([core_sum[n] for n in late], "swap_sibling_early")))
    sent_last, out_d["norm_g"], theirs = lax.optimization_barrier((sent_last, out_d["norm_g"], theirs))
    core_sum["w_in"] = lax.switch(me[0], [
        lambda: _sum_block(own["block0"], sent_last[0]),
        lambda: jnp.concatenate([_sum_block(own["head1"], sent_last[1]), _sum_block(own["tail1"], sent[0])], axis=1),
        lambda: _sum_block(own["block2"], sent[1]),
        lambda: _sum_block(own["block3"], sent[2])])
    core_sum.update({n: _sum4(own[n], r, me) for n, r in zip(loras, sent_last[2:])})
    rest = ("w_in",) + loras
    theirs.update(zip(rest, _swap_sibling([core_sum[n] for n in rest], "swap_sibling")))
    for n in SHARDED:
        m_n, v_n = moments if n == "w_in" else (m[n][0], v[n][0])
        g, d, m2, v2 = _adamw(w[n][0], m_n, v_n, [core_sum[n], theirs[n]], "adamw_" + n)
        out_g[n], out_d[n], out_m[n], out_v[n] = (a.reshape(shapes[n]) for a in (g, d, m2, v2))

    return (loss, grad_x.reshape(x.shape), *[out_g[n] for n in WEIGHTS], *[out_d[n] for n in WEIGHTS],
            *[out_m[n] for n in WEIGHTS], *[out_v[n] for n in WEIGHTS])
```

```python
import functools
import math

import jax
import jax.numpy as jnp
from jax import lax
from jax.experimental import pallas as pl
from jax.experimental.pallas import tpu as pltpu
from jax.experimental.pallas import tpu_sc as plsc

F32 = jnp.float32
BF16 = jnp.bfloat16

D_MODEL = 1024
D_HALF = 512
HEAD = 64
N_HEADS = 8
LORA = 64
RWKV_COLS = 2176
FOX_REAL = 2056
SEC = 2176
GATE_COLS = 2048
IN_COLS = 6280
N_CHIPS = 4
SHARD_COLS = IN_COLS // N_CHIPS
A_TAIL = RWKV_COLS - SHARD_COLS
B_HEAD = SHARD_COLS - A_TAIL
B_TAIL = FOX_REAL - B_HEAD
G_HEAD = SHARD_COLS - B_TAIL
RMS_EPS = 1e-6
LNX_EPS = 64e-5
ATT_SCALE = HEAD ** -0.5
NEG = -1e30

ADAM_LR = 0.001
ADAM_B1 = 0.9
ADAM_B2 = 0.999
ADAM_EPS = 1e-08
ADAM_WD = 0.01
ADAM_STEP = 10

LANES = 128
SUBLANES = 8
VMEM_LIMIT = 56 * 1024 * 1024
MESH = pl.DeviceIdType.MESH


def _params(*sem):
    return pltpu.CompilerParams(dimension_semantics=sem if sem else None, vmem_limit_bytes=VMEM_LIMIT)


def _sigmoid(x):
    return 1.0 / (1.0 + jnp.exp(-x))


def _log_sigmoid(x):
    return jnp.minimum(x, 0.0) - jnp.log(1.0 + jnp.exp(-jnp.abs(x)))


def _head_ones():
    r = lax.broadcasted_iota(jnp.int32, (LANES, LANES), 0) >> 6
    c = lax.broadcasted_iota(jnp.int32, (LANES, LANES), 1) >> 6
    return (r == c).astype(BF16)


def _split3(x):
    hi = x.astype(BF16)
    r1 = x - hi.astype(F32)
    mid = r1.astype(BF16)
    lo = (r1 - mid.astype(F32)).astype(BF16)
    return hi, mid, lo


def _exact_dot(x, ones_bf16, ones_first=False):
    out = None
    for piece in _split3(x):
        if ones_first:
            t = jnp.dot(ones_bf16, piece, preferred_element_type=F32)
        else:
            t = jnp.dot(piece, ones_bf16, preferred_element_type=F32)
        out = t if out is None else out + t
    return out


def _head_sum(x, bd):
    n = x.shape[1] // LANES
    parts = [_exact_dot(x[:, i * LANES:(i + 1) * LANES], bd) for i in range(n)]
    return parts[0] if n == 1 else jnp.concatenate(parts, axis=1)


def _dot_nt(a, b):
    return lax.dot_general(a, b, (((1,), (1,)), ((), ())), preferred_element_type=F32)


def _dot_tn(a, b):
    return lax.dot_general(a, b, (((0,), (0,)), ((), ())), preferred_element_type=F32)


def _colsum(x):
    return jnp.sum(x, axis=0, keepdims=True)


def _rmsnorm_in(x, g, tm=512):
    s, d = x.shape

    def body(x_ref, g_ref, h_ref):
        xv = x_ref[...]
        r = lax.rsqrt(jnp.mean(xv * xv, axis=-1, keepdims=True) + RMS_EPS)
        h_ref[...] = (xv * r * g_ref[...]).astype(BF16)

    return pl.pallas_call(
        body, name="rmsnorm_in", grid=(s // tm,),
        in_specs=[pl.BlockSpec((tm, d), lambda i: (i, 0)), pl.BlockSpec((1, d), lambda i: (0, 0))],
        out_specs=pl.BlockSpec((tm, d), lambda i: (i, 0)),
        out_shape=jax.ShapeDtypeStruct((s, d), BF16), compiler_params=_params("parallel"),
    )(x, g)


def _matmul_nn(a, b, name, tm=512):
    m, k = a.shape
    n = b.shape[1]

    def body(a_ref, b_ref, o_ref):
        o_ref[...] = jnp.dot(a_ref[...], b_ref[...], preferred_element_type=F32)

    return pl.pallas_call(
        body, name=name, grid=(m // tm,),
        in_specs=[pl.BlockSpec((tm, k), lambda i: (i, 0)), pl.BlockSpec((k, n), lambda i: (0, 0))],
        out_specs=pl.BlockSpec((tm, n), lambda i: (i, 0)),
        out_shape=jax.ShapeDtypeStruct((m, n), F32), compiler_params=_params("parallel"),
    )(a, b)


def _matmul_tn_acc(at, b, name, tk=512):
    m, k = at.shape
    n = b.shape[1]

    def body(a_ref, b_ref, o_ref):
        j = pl.program_id(0)

        @pl.when(j == 0)
        def _():
            o_ref[...] = jnp.zeros_like(o_ref)

        o_ref[...] += jnp.dot(a_ref[...], b_ref[...].astype(BF16), preferred_element_type=F32)

    return pl.pallas_call(
        body, name=name, grid=(k // tk,),
        in_specs=[pl.BlockSpec((m, tk), lambda j: (0, j)), pl.BlockSpec((tk, n), lambda j: (j, 0))],
        out_specs=pl.BlockSpec((m, n), lambda j: (0, 0)),
        out_shape=jax.ShapeDtypeStruct((m, n), F32), compiler_params=_params("arbitrary"),
    )(at, b)


def _inproj_bwd(du_a, du_b, du_g, w_a, w_b, w_g, x, dx2, g, exchange=None, tm=256):
    s, d = x.shape
    nb = s // tm

    def body(*refs):
        ((da_ref, db_ref, dg_ref, wa_ref, wb_ref, wg_ref, x_ref, dx2_ref, g_ref), (gx_ref, gg_ref), _,
         moves) = _split_refs(refs, 9, 2, exchange)
        i = pl.program_id(0)
        if moves:
            moves.start(also=(i == 0))

        @pl.when(i == 0)
        def _():
            gg_ref[...] = jnp.zeros_like(gg_ref)

        dh = _dot_nt(da_ref[...].astype(BF16), wa_ref[...])
        dh += _dot_nt(db_ref[...].astype(BF16), wb_ref[...])
        dh += _dot_nt(dg_ref[...].astype(BF16), wg_ref[...])
        xv = x_ref[...]
        r = lax.rsqrt(jnp.mean(xv * xv, axis=-1, keepdims=True) + RMS_EPS)
        xh = xv * r
        gg_ref[...] += _colsum(dh * xh)
        dxh = dh * g_ref[...]
        gx_ref[...] = dx2_ref[...] + r * (dxh - xh * jnp.mean(dxh * xh, axis=-1, keepdims=True))
        if moves:
            moves.wait(also=(i == nb - 1))

    row = lambda w: pl.BlockSpec((tm, w), lambda i: (i, 0))
    full = lambda a: pl.BlockSpec(a.shape, lambda i: (0, 0))
    ex_in = exchange.operands if exchange else []
    ex_out = exchange.out_shapes if exchange else []
    res = pl.pallas_call(
        body, name="inproj_bwd", grid=(nb,),
        in_specs=[row(SEC), row(SEC), row(GATE_COLS), full(w_a), full(w_b), full(w_g), row(d), row(d), full(g)]
                 + [ANY] * len(ex_in),
        out_specs=[row(d), pl.BlockSpec((1, d), lambda i: (0, 0))] + [ANY] * len(ex_out),
        out_shape=[jax.ShapeDtypeStruct((s, d), F32), jax.ShapeDtypeStruct((1, d), F32)] + ex_out,
        scratch_shapes=exchange.scratch() if exchange else [],
        compiler_params=_params("arbitrary"),
    )(du_a, du_b, du_g, w_a, w_b, w_g, x, dx2, g, *ex_in)
    return res[0], res[1], list(res[2:])


def _rwkv_elementwise(ua, prev_row, first, mu, wl, w0, a0, kkw, kaw, bd):
    tm = ua.shape[0]
    rows = lax.broadcasted_iota(jnp.int32, (tm, 1), 0)
    prev = jnp.where(first, jnp.zeros_like(prev_row), prev_row)
    shifted = jnp.where(rows == 0, prev, pltpu.roll(ua, 1, 0))
    delta = shifted - ua
    us = ua + delta * mu
    r = us[:, 0:512]
    k0 = us[:, 512:1024]
    v = us[:, 1024:1536]
    lo = us[:, 1536:1664]
    gate = us[:, 1664:2176]
    lane = lax.broadcasted_iota(jnp.int32, (1, LANES), 1)
    th = jnp.tanh(lo)
    lin = jnp.where(lane < LORA, th, lo)
    ll = jnp.dot(lin.astype(BF16), wl, preferred_element_type=F32)
    sz = _sigmoid(w0 + ll[:, :512])
    e = sz * math.exp(-0.5)
    dec = jnp.exp(-e)
    a = _sigmoid(a0 + ll[:, 512:])
    kk0 = k0 * kkw
    ss = _head_sum(kk0 * kk0, bd)
    nrm = jnp.maximum(jnp.sqrt(ss), 1e-12)
    kk = kk0 / nrm
    k = k0 * (1.0 + (a - 1.0) * kaw)
    return dict(delta=delta, us=us, r=r, k0=k0, v=v, lo=lo, gate=gate, th=th, lin=lin, sz=sz, e=e, dec=dec,
                a=a, kk0=kk0, ss=ss, nrm=nrm, kk=kk, k=k)


def _rwkv_prep(u_a, mu, wl, w0, a0, kkw, kaw, tm=256):
    s = u_a.shape[0]

    def body(ua_ref, prev_ref, mu_ref, wl_ref, w0_ref, a0_ref, kkw_ref, kaw_ref,
             r_ref, w_ref, k_ref, v_ref, a_ref, b_ref, g_ref):
        i = pl.program_id(0)
        f = _rwkv_elementwise(ua_ref[...], prev_ref[7:8, :], i == 0, mu_ref[...], wl_ref[...], w0_ref[...],
                              a0_ref[...], kkw_ref[...], kaw_ref[...], _head_ones())
        r_ref[...] = f["r"]
        w_ref[...] = f["dec"]
        k_ref[...] = f["k"]
        v_ref[...] = f["v"]
        a_ref[...] = -f["kk"]
        b_ref[...] = f["kk"] * f["a"]
        g_ref[...] = f["gate"]

    vec = lambda w: pl.BlockSpec((1, w), lambda i: (0, 0))
    out = pl.BlockSpec((tm, D_HALF), lambda i: (i, 0))
    return pl.pallas_call(
        body, name="rwkv_prep", grid=(s // tm,),
        in_specs=[pl.BlockSpec((tm, SEC), lambda i: (i, 0)),
                  pl.BlockSpec((8, SEC), lambda i: (jnp.maximum(i * (tm // 8) - 1, 0), 0)),
                  vec(SEC), pl.BlockSpec((LANES, 2 * D_HALF), lambda i: (0, 0)),
                  vec(D_HALF), vec(D_HALF), vec(D_HALF), vec(D_HALF)],
        out_specs=[out] * 7,
        out_shape=[jax.ShapeDtypeStruct((s, D_HALF), F32)] * 7,
        compiler_params=_params("parallel"),
    )(u_a, u_a, mu, wl, w0, a0, kkw, kaw)


SCAN_TB = 128
N_PAIRS = 4


def _pair_sum(x, left):
    s_l = jnp.sum(jnp.where(left, x, 0.0), axis=1, keepdims=True)
    s_r = jnp.sum(jnp.where(left, 0.0, x), axis=1, keepdims=True)
    return jnp.where(left, s_l, s_r)


def _pair_dot(x, row_l, row_r, left):
    s_l = jnp.sum(x * row_l, axis=1, keepdims=True)
    s_r = jnp.sum(x * row_r, axis=1, keepdims=True)
    return jnp.where(left, s_l, s_r)


def _halves(rows8):
    lane = lax.broadcasted_iota(jnp.int32, rows8.shape, 1)
    keep_left = (lane & (LANES - 1)) < HEAD
    return jnp.where(keep_left, rows8, 0.0), jnp.where(keep_left, 0.0, rows8)


def _quad_consts():
    lane = lax.broadcasted_iota(jnp.int32, (HEAD, 2 * LANES), 1)
    rowi = lax.broadcasted_iota(jnp.int32, (HEAD, 2 * LANES), 0)
    diag2 = rowi == (lane & (HEAD - 1))
    r = lax.broadcasted_iota(jnp.int32, (2 * LANES, 2 * LANES), 0) >> 6
    c = lax.broadcasted_iota(jnp.int32, (2 * LANES, 2 * LANES), 1) >> 6
    return diag2, (r == c).astype(BF16)


def _rows_to_columns(x8, diag2, bd2):
    lhs = jnp.concatenate([jnp.where(diag2, x8[i:i + 1], 0.0).astype(BF16) for i in range(SUBLANES)], axis=0)
    return jnp.dot(lhs, bd2, preferred_element_type=F32)


def _diag_rows(qtile, diag2, bd2, sub_row2):
    res = jnp.dot(qtile, bd2, preferred_element_type=F32)
    out = jnp.zeros((SUBLANES, 2 * LANES), F32)
    for i in range(SUBLANES):
        out = jnp.where(sub_row2 == i, _colsum(jnp.where(diag2, res[i * HEAD:(i + 1) * HEAD], 0.0)), out)
    return out


def _store_tile(qbuf, slot, p, i, x):
    qbuf[slot, p // 2, i * HEAD:(i + 1) * HEAD, (p % 2) * LANES:(p % 2 + 1) * LANES] = x.astype(BF16)


def _left_half():
    return lax.broadcasted_iota(jnp.int32, (HEAD, LANES), 1) < HEAD


def _split_refs(refs, n_rows, n_out, exchange):
    n_in = len(exchange.operands) if exchange else 0
    n_ex_out = len(exchange.out_shapes) if exchange else 0
    refs = list(refs)
    rows, refs = refs[:n_rows], refs[n_rows:]
    ex_in, refs = refs[:n_in], refs[n_in:]
    outs, refs = refs[:n_out], refs[n_out:]
    ex_out, refs = refs[:n_ex_out], refs[n_ex_out:]
    scratch, sems = (refs[:-3], refs[-3:]) if exchange else (refs, None)
    moves = exchange.moves(ex_in, ex_out, sems) if exchange else None
    return rows, outs, scratch, moves


def _wkv_fwd(r, w, k, a, b, v, exchange=None):
    s = r.shape[0]
    tb = SCAN_TB
    nb = s // tb

    def body(*refs):
        (r_ref, w_ref, k_ref, a_ref, b_ref, v_ref), (y_ref, st_ref), (state, vbuf, qbuf), moves = _split_refs(
            refs, 6, 2, exchange)
        g = pl.program_id(0)
        if moves:
            moves.start(also=(g == 0))

        @pl.when(g == 0)
        def _():
            state[...] = jnp.zeros_like(state)
            qbuf[...] = jnp.zeros_like(qbuf)

        left = _left_half()
        diag2, bd2 = _quad_consts()
        sub_row2 = lax.broadcasted_iota(jnp.int32, (SUBLANES, 2 * LANES), 0)
        groups = tb // SUBLANES
        quads = [slice(g2 * 2 * LANES, (g2 + 1) * 2 * LANES) for g2 in range(2)]

        def rows_of(q):
            return pl.ds(pl.multiple_of(q * SUBLANES, SUBLANES), SUBLANES)

        def v_tiles(q, slot):
            v8 = v_ref[rows_of(q), :]
            for g2 in range(2):
                vbuf[slot, g2] = _rows_to_columns(v8[:, quads[g2]], diag2, bd2)

        def chain(q, slot):
            rows8 = rows_of(q)
            a8, w8, b8, k8, r8 = (x[rows8, :] for x in (a_ref, w_ref, b_ref, k_ref, r_ref))
            pairs = [slice(p * LANES, (p + 1) * LANES) for p in range(N_PAIRS)]
            a_next = pltpu.roll(a8, SUBLANES - 1, 0)
            (a8_l, a8_r), (wa8_l, wa8_r) = _halves(a8), _halves(w8 * a_next)
            ba8 =jnp.concatenate([_pair_sum(b8[:, pr] * a_next[:, pr], left[0:SUBLANES]) for pr in pairs], axis=1)
            ka8 = jnp.concatenate([_pair_sum(k8[:, pr] * a_next[:, pr], left[0:SUBLANES]) for pr in pairs], axis=1)
            sp = [state[p] for p in range(N_PAIRS)]
            for i in range(0, SUBLANES, 2):
                r0, r1 = slice(i, i + 1), slice(i + 1, i + 2)
                sums = [(_pair_dot(sp[p], a8_l[r0, pairs[p]], a8_r[r0, pairs[p]], left),
                         _pair_dot(sp[p], wa8_l[r0, pairs[p]], wa8_r[r0, pairs[p]], left)) for p in range(N_PAIRS)]
                sa0, sa1 = [s[0] for s in sums], [s[1] for s in sums]
                for p in range(N_PAIRS):
                    pr = pairs[p]
                    inner = slice((p % 2) * LANES, (p % 2 + 1) * LANES)
                    vt0 = vbuf[slot, p // 2, i * HEAD:(i + 1) * HEAD, inner]
                    vt1 = vbuf[slot, p // 2, (i + 1) * HEAD:(i + 2) * HEAD, inner]
                    sa_next = sa1[p] + sa0[p] * ba8[r0, pr] + vt0 * ka8[r0, pr]
                    s1 = sp[p] * w8[r0, pr] + sa0[p] * b8[r0, pr] + vt0 * k8[r0, pr]
                    st_ref[q * SUBLANES + i, p] = s1
                    _store_tile(qbuf, slot, p, i, s1 * r8[r0, pr])
                    s2 = s1 * w8[r1, pr] + sa_next * b8[r1, pr] + vt1 * k8[r1, pr]
                    st_ref[q * SUBLANES + i + 1, p] = s2
                    _store_tile(qbuf, slot, p, i + 1, s2 * r8[r1, pr])
                    sp[p] = s2
            for p in range(N_PAIRS):
                state[p] = sp[p]

        def y_rows(q, slot):
            for g2 in range(2):
                y_ref[rows_of(q), quads[g2]] = _diag_rows(qbuf[slot, g2], diag2, bd2, sub_row2)

        v_tiles(0, 0)

        def two_groups(j, carry):
            q0 = 2 * j
            v_tiles(q0 + 1, 1)
            chain(q0, 0)
            y_rows(jnp.maximum(q0 - 1, 0), 1)
            v_tiles(jnp.minimum(q0 + 2, groups - 1), 0)
            chain(q0 + 1, 1)
            y_rows(q0, 0)
            return carry

        lax.fori_loop(0, groups // 2, two_groups, 0)
        y_rows(groups - 1, 1)
        if moves:
            moves.wait(also=(g == nb - 1))

    rows = pl.BlockSpec((tb, D_HALF), lambda g: (g, 0))
    ex_in = exchange.operands if exchange else []
    ex_out = exchange.out_shapes if exchange else []
    res = pl.pallas_call(
        body, name="wkv_fwd", grid=(nb,),
        in_specs=[rows] * 6 + [ANY] * len(ex_in),
        out_specs=[rows, pl.BlockSpec((tb, N_PAIRS, HEAD, LANES), lambda g: (g, 0, 0, 0))] + [ANY] * len(ex_out),
        out_shape=[jax.ShapeDtypeStruct((s, D_HALF), F32),
                   jax.ShapeDtypeStruct((s, N_PAIRS, HEAD, LANES), F32)] + ex_out,
        scratch_shapes=[pltpu.VMEM((N_PAIRS, HEAD, LANES), F32),
                        pltpu.VMEM((2, 2, SUBLANES * HEAD, 2 * LANES), F32),
                        pltpu.VMEM((2, 2, SUBLANES * HEAD, 2 * LANES), BF16)]
                       + (exchange.scratch() if exchange else []),
        compiler_params=_params("arbitrary"),
    )(r, w, k, a, b, v, *ex_in)
    return res[0], res[1], list(res[2:])


def _wkv_bwd(r, w, k, a, b, v, dy, st, exchange=None):
    s = r.shape[0]
    tb = SCAN_TB
    nb = s // tb

    def body(*refs):
        ((r_ref, w_ref, k_ref, a_ref, b_ref, v_ref, dy_ref, st_ref, before_ref),
         (dr_ref, dw_ref, dk_ref, dv_ref, da_ref, db_ref), (dstate, vbuf, qbuf, sbuf),
         moves) = _split_refs(refs, 9, 6, exchange)
        g = pl.program_id(0)
        first_block = g == nb - 1
        if moves:
            moves.start(also=(g == 0))

        @pl.when(g == 0)
        def _():
            dstate[...] = jnp.zeros_like(dstate)
            qbuf[...] = jnp.zeros_like(qbuf)

        left = _left_half()
        diag2, bd2 = _quad_consts()
        sub_row = lax.broadcasted_iota(jnp.int32, (SUBLANES, LANES), 0)
        sub_row2 = lax.broadcasted_iota(jnp.int32, (SUBLANES, 2 * LANES), 0)
        groups = tb // SUBLANES
        quads = [slice(g2 * 2 * LANES, (g2 + 1) * 2 * LANES) for g2 in range(2)]
        row_refs = (dr_ref, dw_ref, dk_ref, da_ref, db_ref)

        def rows_of(q):
            return pl.ds(pl.multiple_of(q * SUBLANES, SUBLANES), SUBLANES)

        def state_before(q, i, p):
            if i > 0:
                return st_ref[q * SUBLANES + i - 1, p]
            return jnp.where(q == 0, jnp.where(first_block, 0.0, before_ref[0, p]),
                             st_ref[jnp.maximum(q * SUBLANES - 1, 0), p])

        def column_tiles(q, slot):
            rows8 = rows_of(q)
            for kind, ref in enumerate((v_ref, dy_ref)):
                x8 = ref[rows8, :]
                for g2 in range(2):
                    vbuf[slot, kind, g2] = _rows_to_columns(x8[:, quads[g2]], diag2, bd2)
            a8 = a_ref[rows8, :]
            for i in range(SUBLANES):
                for p in range(N_PAIRS):
                    _store_tile(sbuf, 0, p, i, state_before(q, i, p) * a8[i:i + 1, p * LANES:(p + 1) * LANES])
            for g2 in range(2):
                vbuf[slot, 2, g2] = jnp.dot(sbuf[0, g2], bd2, preferred_element_type=F32)

        def chain(q, slot):
            rows8 = rows_of(q)
            a8, w8, b8, k8, r8 = (x[rows8, :] for x in (a_ref, w_ref, b_ref, k_ref, r_ref))
            b8_l, b8_r = _halves(b8)
            dsp = [dstate[p] for p in range(N_PAIRS)]
            outs = [[jnp.zeros((SUBLANES, LANES), F32) for _ in row_refs] for _ in range(N_PAIRS)]
            after = [st_ref[q * SUBLANES + SUBLANES - 1, p] for p in range(N_PAIRS)]
            for i in reversed(range(SUBLANES)):
                row = slice(i, i + 1)
                pl_ = [slice(p * LANES, (p + 1) * LANES) for p in range(N_PAIRS)]
                tile = [(p // 2, slice(i * HEAD, (i + 1) * HEAD), slice((p % 2) * LANES, (p % 2 + 1) * LANES))
                        for p in range(N_PAIRS)]
                sp = [state_before(q, i, p) for p in range(N_PAIRS)]
                dyt = [vbuf[(slot, 1) + tile[p]] for p in range(N_PAIRS)]
                ds = [dsp[p] + dyt[p] * r8[row, pl_[p]] for p in range(N_PAIRS)]
                dsa = [_pair_dot(ds[p], b8_l[row, pl_[p]], b8_r[row, pl_[p]], left) for p in range(N_PAIRS)]
                sa = [vbuf[(slot, 2) + tile[p]] for p in range(N_PAIRS)]
                for p in range(N_PAIRS):
                    ar, wr, br, kr = (x[row, pl_[p]] for x in (a8, w8, b8, k8))
                    vt = vbuf[(slot, 0) + tile[p]]
                    dsp[p] = ds[p] * wr + dsa[p] * ar
                    new = (_colsum(after[p] * dyt[p]), _colsum(ds[p] * sp[p]), _colsum(ds[p] * vt),
                           _colsum(sp[p] * dsa[p]), _colsum(ds[p] * sa[p]))
                    outs[p] = [jnp.where(sub_row == i, n, o) for n, o in zip(new, outs[p])]
                    _store_tile(qbuf, slot, p, i, ds[p] * kr)
                after = sp
            for p in range(N_PAIRS):
                dstate[p] = dsp[p]
                for ref, o in zip(row_refs, outs[p]):
                    ref[rows8, p * LANES:(p + 1) * LANES] = o

        def dv_rows(q, slot):
            for g2 in range(2):
                dv_ref[rows_of(q), quads[g2]] = _diag_rows(qbuf[slot, g2], diag2, bd2, sub_row2)

        column_tiles(groups - 1, 0)

        def two_groups(j, carry):
            q0 = groups - 1 - 2 * j
            column_tiles(q0 - 1, 1)
            chain(q0, 0)
            dv_rows(jnp.minimum(q0 + 1, groups - 1), 1)
            column_tiles(jnp.maximum(q0 - 2, 0), 0)
            chain(q0 - 1, 1)
            dv_rows(q0, 0)
            return carry

        lax.fori_loop(0, groups // 2, two_groups, 0)
        dv_rows(0, 1)
        if moves:
            moves.wait(also=(g == nb - 1))

    rows = pl.BlockSpec((tb, D_HALF), lambda g: (nb - 1 - g, 0))
    ex_in = exchange.operands if exchange else []
    ex_out = exchange.out_shapes if exchange else []
    res = pl.pallas_call(
        body, name="wkv_bwd", grid=(nb,),
        in_specs=[rows] * 7 + [pl.BlockSpec((tb, N_PAIRS, HEAD, LANES), lambda g: (nb - 1 - g, 0, 0, 0)),
                               pl.BlockSpec((1, N_PAIRS, HEAD, LANES),
                                            lambda g: (jnp.maximum((nb - 1 - g) * tb - 1, 0), 0, 0, 0))]
                 + [ANY] * len(ex_in),
        out_specs=[rows] * 6 + [ANY] * len(ex_out),
        out_shape=[jax.ShapeDtypeStruct((s, D_HALF), F32)] * 6 + ex_out,
        scratch_shapes=[pltpu.VMEM((N_PAIRS, HEAD, LANES), F32),
                        pltpu.VMEM((2, 3, 2, SUBLANES * HEAD, 2 * LANES), F32),
                        pltpu.VMEM((2, 2, SUBLANES * HEAD, 2 * LANES), BF16),
                        pltpu.VMEM((1, 2, SUBLANES * HEAD, 2 * LANES), BF16)]
                       + (exchange.scratch() if exchange else []),
        compiler_params=_params("arbitrary"),
    )(r, w, k, a, b, v, dy, st, st, *ex_in)
    return list(res[:6]), list(res[6:])


def _rwkv_post_math(y, r, k, v, gate, lw, lb, rk, bd):
    mean = _head_sum(y, bd) * (1.0 / HEAD)
    yc = y - mean
    var = _head_sum(yc * yc, bd) * (1.0 / HEAD)
    rstd = lax.rsqrt(var + LNX_EPS)
    yn = yc * rstd
    rkk = _head_sum(r * k * rk, bd)
    sg = _sigmoid(gate)
    pre = yn * lw + lb + rkk * v
    return yn, rstd, rkk, sg, pre


def _rwkv_prep_bwd(u_a, grads, mu, wl, w0, a0, kkw, kaw, tm=256):
    s = u_a.shape[0]
    nb = s // tm

    def body(ua_ref, prev_ref, drs_ref, dws_ref, dks_ref, dvs_ref, das_ref, dbs_ref, drb_ref, dkb_ref, dvb_ref,
             dgt_ref, mu_ref, wl_ref, w0_ref, a0_ref, kkw_ref, kaw_ref,
             du_ref, dmu_ref, dwl_ref, dw0_ref, da0_ref, dkkw_ref, dkaw_ref, carry):
        i = pl.program_id(0)

        @pl.when(i == 0)
        def _():
            carry[...] = jnp.zeros_like(carry)
            for ref in (dmu_ref, dwl_ref, dw0_ref, da0_ref, dkkw_ref, dkaw_ref):
                ref[...] = jnp.zeros_like(ref)

        bd = _head_ones()
        mu_v, wl_v, kkw_v, kaw_v = mu_ref[...], wl_ref[...], kkw_ref[...], kaw_ref[...]
        f = _rwkv_elementwise(ua_ref[...], prev_ref[7:8, :], i == nb - 1, mu_v, wl_v, w0_ref[...],
                              a0_ref[...], kkw_v, kaw_v, bd)
        a, kk, k0 = f["a"], f["kk"], f["k0"]
        dk = dks_ref[...] + dkb_ref[...]
        dbs = dbs_ref[...]
        dkk = dbs * a - das_ref[...]
        da = dbs * kk + dk * k0 * kaw_v
        dk0 = dk * (1.0 + (a - 1.0) * kaw_v)
        dkaw_ref[...] += _colsum(dk * k0 * (a - 1.0))
        inv = 1.0 / f["nrm"]
        proj = _head_sum(dkk * kk, bd)
        dkk0 = jnp.where(f["ss"] > 1e-24, (dkk - kk * proj) * inv, dkk * inv)
        dk0 = dk0 + dkk0 * kkw_v
        dkkw_ref[...] += _colsum(dkk0 * k0)
        dza = da * a * (1.0 - a)
        da0_ref[...] += _colsum(dza)
        dz = -dws_ref[...] * f["dec"] * f["e"] * (1.0 - f["sz"])
        dw0_ref[...] += _colsum(dz)
        dll = jnp.concatenate([dz, dza], axis=1).astype(BF16)
        dwl_ref[...] += _dot_tn(f["lin"].astype(BF16), dll)
        dlin = _dot_nt(dll, wl_v)
        lane = lax.broadcasted_iota(jnp.int32, (1, LANES), 1)
        th = f["th"]
        dlo = jnp.where(lane < LORA, dlin * (1.0 - th * th), dlin)
        dus = jnp.concatenate([drs_ref[...] + drb_ref[...], dk0, dvs_ref[...] + dvb_ref[...], dlo, dgt_ref[...]],
                              axis=1)
        dmu_ref[...] += _colsum(dus * f["delta"])
        g1 = dus * mu_v
        rows = lax.broadcasted_iota(jnp.int32, (tm, 1), 0)
        up = jnp.where(rows == tm - 1, carry[...], pltpu.roll(g1, tm - 1, 0))
        du_ref[...] = dus - g1 + up
        carry[...] = g1[0:1, :]

    rev = lambda w: pl.BlockSpec((tm, w), lambda i: (nb - 1 - i, 0))
    vec = lambda w: pl.BlockSpec((1, w), lambda i: (0, 0))
    wl_spec = pl.BlockSpec((LANES, 2 * D_HALF), lambda i: (0, 0))
    return pl.pallas_call(
        body, name="rwkv_prep_bwd", grid=(nb,),
        in_specs=[rev(SEC), pl.BlockSpec((8, SEC), lambda i: (jnp.maximum((nb - 1 - i) * (tm // 8) - 1, 0), 0))]
                 + [rev(D_HALF)] * 10 + [vec(SEC), wl_spec] + [vec(D_HALF)] * 4,
        out_specs=[rev(SEC), vec(SEC), wl_spec] + [vec(D_HALF)] * 4,
        out_shape=[jax.ShapeDtypeStruct((s, SEC), F32), jax.ShapeDtypeStruct((1, SEC), F32),
                   jax.ShapeDtypeStruct((LANES, 2 * D_HALF), F32)] + [jax.ShapeDtypeStruct((1, D_HALF), F32)] * 4,
        scratch_shapes=[pltpu.VMEM((1, SEC), F32)],
        compiler_params=_params("arbitrary"),
    )(u_a, u_a, *grads, mu, wl, w0, a0, kkw, kaw)


def _tri(tm, lower):
    r = lax.broadcasted_iota(jnp.int32, (tm, tm), 0)
    c = lax.broadcasted_iota(jnp.int32, (tm, tm), 1)
    return ((r >= c) if lower else (r <= c)).astype(BF16)


def _head_rms(x, g, bd):
    rinv = lax.rsqrt(_head_sum(x * x, bd) * (1.0 / HEAD) + RMS_EPS)
    xh = x * rinv
    return xh, rinv, xh * g


def _fox_prep(u_b, fb, qg, kg, tm=256):
    s = u_b.shape[0]

    def body(ub_ref, fb_ref, qg_ref, kg_ref, q_ref, k_ref, v_ref, cc_ref, cr_ref, carry):
        i = pl.program_id(0)

        @pl.when(i == 0)
        def _():
            carry[...] = jnp.zeros_like(carry)

        bd = _head_ones()
        _, _, qn = _head_rms(ub_ref[:, 0:512], qg_ref[...], bd)
        _, _, kn = _head_rms(ub_ref[:, 512:1024], kg_ref[...], bd)
        q_ref[...] = (qn * ATT_SCALE).astype(BF16)
        k_ref[...] = kn.astype(BF16)
        v_ref[...] = ub_ref[:, 1024:1536].astype(BF16)
        lane = lax.broadcasted_iota(jnp.int32, (1, LANES), 1)
        logf = jnp.where(lane < N_HEADS, _log_sigmoid(ub_ref[:, 2048:2176] + fb_ref[...]), 0.0)
        cum = _exact_dot(logf, _tri(tm, True), ones_first=True) + carry[...]
        for h in range(N_HEADS):
            cc_ref[h] = jnp.broadcast_to(cum[:, h:h + 1], (tm, LANES))
        cr_ref[...] = jnp.transpose(cum)[0:N_HEADS, :]
        carry[...] = cum[tm - 1:tm, :]

    blk = pl.BlockSpec((tm, D_HALF), lambda i: (i, 0))
    return pl.pallas_call(
        body, name="fox_prep", grid=(s // tm,),
        in_specs=[pl.BlockSpec((tm, SEC), lambda i: (i, 0)), pl.BlockSpec((1, LANES), lambda i: (0, 0)),
                  pl.BlockSpec((1, D_HALF), lambda i: (0, 0)), pl.BlockSpec((1, D_HALF), lambda i: (0, 0))],
        out_specs=[blk, blk, blk, pl.BlockSpec((N_HEADS, tm, LANES), lambda i: (0, i, 0)),
                   pl.BlockSpec((N_HEADS, tm), lambda i: (0, i))],
        out_shape=[jax.ShapeDtypeStruct((s, D_HALF), BF16)] * 3
                  + [jax.ShapeDtypeStruct((N_HEADS, s, LANES), F32), jax.ShapeDtypeStruct((N_HEADS, s), F32)],
        scratch_shapes=[pltpu.VMEM((1, LANES), F32)],
        compiler_params=_params("arbitrary"),
    )(u_b, fb, qg, kg)


ATT_T = 256


def _attn_fwd(q, k, v, cc, cr):
    s = q.shape[0]
    t = ATT_T
    nblk = s // t

    def body(q_ref, k_ref, v_ref, cc_ref, cr_ref, o_ref, lse_ref, m_sc, l_sc, acc_sc):
        i = pl.program_id(0)
        j = pl.program_id(1)

        @pl.when(j == 0)
        def _():
            m_sc[...] = jnp.full_like(m_sc, NEG)
            l_sc[...] = jnp.zeros_like(l_sc)
            acc_sc[...] = jnp.zeros_like(acc_sc)

        def tile(on_diagonal):
            causal = _causal_tile(t) if on_diagonal else None
            left = lax.broadcasted_iota(jnp.int32, (1, LANES), 1) < HEAD
            for p in range(N_PAIRS):
                lanes = slice(p * LANES, (p + 1) * LANES)
                q2, k2, v2 = q_ref[:, lanes], k_ref[:, lanes], v_ref[:, lanes]
                acc2 = acc_sc[:, lanes]
                for e in range(2):
                    h = 2 * p + e
                    msk = left if e == 0 else jnp.logical_not(left)
                    sc = _dot_nt(jnp.where(msk, q2, jnp.zeros_like(q2)), k2)
                    sc = sc + (_wide(cc_ref[h]) - cr_ref[h:h + 1, :])
                    if on_diagonal:
                        sc = jnp.where(causal, sc, NEG)
                    m_prev = m_sc[h]
                    m_new = jnp.maximum(m_prev, jnp.max(sc, axis=1, keepdims=True))
                    alpha = jnp.exp(m_prev - m_new)
                    pm = jnp.exp(sc - _wide(m_new))
                    l_sc[h] = alpha * l_sc[h] + jnp.sum(pm, axis=1, keepdims=True)
                    m_sc[h] = m_new
                    pv = jnp.dot(pm.astype(BF16), v2, preferred_element_type=F32)
                    acc2 = jnp.where(msk, alpha * acc2 + pv, acc2)
                acc_sc[:, lanes] = acc2

        pl.when(j < i)(functools.partial(tile, False))
        pl.when(j == i)(functools.partial(tile, True))

        @pl.when(j == i)
        def _():
            left = lax.broadcasted_iota(jnp.int32, (1, LANES), 1) < HEAD
            for p in range(N_PAIRS):
                lanes = slice(p * LANES, (p + 1) * LANES)
                inv = jnp.where(left, 1.0 / l_sc[2 * p], 1.0 / l_sc[2 * p + 1])
                o_ref[:, lanes] = acc_sc[:, lanes] * inv
            for h in range(N_HEADS):
                lse_ref[h] = m_sc[h] + jnp.log(l_sc[h])

    qblk = pl.BlockSpec((t, D_HALF), lambda i, j: (i, 0))
    kblk = pl.BlockSpec((t, D_HALF), lambda i, j: (jnp.minimum(i, j), 0))
    return pl.pallas_call(
        body, name="fox_attn_fwd", grid=(nblk, nblk),
        in_specs=[qblk, kblk, kblk, pl.BlockSpec((N_HEADS, t, LANES), lambda i, j: (0, i, 0)),
                  pl.BlockSpec((N_HEADS, t), lambda i, j: (0, jnp.minimum(i, j)))],
        out_specs=[qblk, pl.BlockSpec((N_HEADS, t, LANES), lambda i, j: (0, i, 0))],
        out_shape=[jax.ShapeDtypeStruct((s, D_HALF), F32), jax.ShapeDtypeStruct((N_HEADS, s, LANES), F32)],
        scratch_shapes=[pltpu.VMEM((N_HEADS, t, LANES), F32), pltpu.VMEM((N_HEADS, t, LANES), F32),
                        pltpu.VMEM((t, D_HALF), F32)],
        compiler_params=_params("parallel", "arbitrary"),
    )(q, k, v, cc, cr)


def _causal_tile(t):
    return lax.broadcasted_iota(jnp.int32, (t, t), 0) >= lax.broadcasted_iota(jnp.int32, (t, t), 1)


def _wide(x):
    return jnp.concatenate([x, x], axis=1)


def _attn_probs(q2, k2, v2, do2, msk, causal, bias, lse_rows):
    zero = jnp.zeros_like(q2)
    qh = jnp.where(msk, q2, zero)
    doh = jnp.where(msk, do2, zero)
    sc = _dot_nt(qh, k2) + bias
    if causal is not None:
        sc = jnp.where(causal, sc, NEG)
    pm = jnp.exp(sc - _wide(lse_rows))
    dp = _dot_nt(doh, v2)
    return qh, doh, pm, dp


def _attn_bwd_rowdot(q, k, v, do, lse, cc, cr):
    s = q.shape[0]
    t = ATT_T
    nblk = s // t

    def body(q_ref, k_ref, v_ref, do_ref, lse_ref, cc_ref, cr_ref, dd_ref, acc):
        i = pl.program_id(0)
        j = pl.program_id(1)

        @pl.when(j == 0)
        def _():
            acc[...] = jnp.zeros_like(acc)

        def tile(on_diagonal):
            causal = _causal_tile(t) if on_diagonal else None
            left = lax.broadcasted_iota(jnp.int32, (1, LANES), 1) < HEAD
            for p in range(N_PAIRS):
                lanes = slice(p * LANES, (p + 1) * LANES)
                q2, k2, v2, do2 = q_ref[:, lanes], k_ref[:, lanes], v_ref[:, lanes], do_ref[:, lanes]
                for e in range(2):
                    h = 2 * p + e
                    msk = left if e == 0 else jnp.logical_not(left)
                    bias = _wide(cc_ref[h]) - cr_ref[h:h + 1, :]
                    _, _, pm, dp = _attn_probs(q2, k2, v2, do2, msk, causal, bias, lse_ref[h])
                    acc[h] += jnp.sum(pm * dp, axis=1, keepdims=True)

        pl.when(j < i)(functools.partial(tile, False))
        pl.when(j == i)(functools.partial(tile, True))

        @pl.when(j == i)
        def _():
            dd_ref[...] = acc[...]

    qblk = pl.BlockSpec((t, D_HALF), lambda i, j: (i, 0))
    qcol = pl.BlockSpec((N_HEADS, t, LANES), lambda i, j: (0, i, 0))
    kblk = pl.BlockSpec((t, D_HALF), lambda i, j: (jnp.minimum(i, j), 0))
    return pl.pallas_call(
        body, name="fox_attn_rowdot", grid=(nblk, nblk),
        in_specs=[qblk, kblk, kblk, qblk, qcol, qcol, pl.BlockSpec((N_HEADS, t), lambda i, j: (0, jnp.minimum(i, j)))],
        out_specs=qcol, out_shape=jax.ShapeDtypeStruct((N_HEADS, s, LANES), F32),
        scratch_shapes=[pltpu.VMEM((N_HEADS, t, LANES), F32)],
        compiler_params=_params("parallel", "arbitrary"),
    )(q, k, v, do, lse, cc, cr)


def _attn_bwd(q, k, v, do, lse, dd, cc, cr):
    s = q.shape[0]
    t = ATT_T
    nblk = s // t

    def body(q_ref, k_ref, v_ref, do_ref, lse_ref, dd_ref, cc_ref, cr_ref,
             dq_ref, dk_ref, dv_ref, dcr_ref, dk_sc, dv_sc, dcr_sc):
        j = pl.program_id(0)
        i = pl.program_id(1)

        @pl.when(jnp.logical_and(j == 0, i == 0))
        def _():
            dq_ref[...] = jnp.zeros_like(dq_ref)

        @pl.when(i == 0)
        def _():
            dk_sc[...] = jnp.zeros_like(dk_sc)
            dv_sc[...] = jnp.zeros_like(dv_sc)
            dcr_sc[...] = jnp.zeros_like(dcr_sc)

        def tile(on_diagonal):
            causal = _causal_tile(t) if on_diagonal else None
            left = lax.broadcasted_iota(jnp.int32, (1, LANES), 1) < HEAD
            qrows = pl.ds(pl.multiple_of(i * t, t), t)
            for p in range(N_PAIRS):
                lanes = slice(p * LANES, (p + 1) * LANES)
                q2, k2, v2, do2 = q_ref[:, lanes], k_ref[:, lanes], v_ref[:, lanes], do_ref[:, lanes]
                zero = jnp.zeros_like(q2)
                dq2 = jnp.zeros((t, LANES), F32)
                dk2 = jnp.zeros((t, LANES), F32)
                dv2 = jnp.zeros((t, LANES), F32)
                for e in range(2):
                    h = 2 * p + e
                    msk = left if e == 0 else jnp.logical_not(left)
                    bias = _wide(cc_ref[h]) - cr_ref[h:h + 1, :]
                    qh, doh, pm, dp = _attn_probs(q2, k2, v2, do2, msk, causal, bias, lse_ref[h])
                    dsc = pm * (dp - _wide(dd_ref[h]))
                    dsb = dsc.astype(BF16)
                    dv2 += _dot_tn(pm.astype(BF16), doh)
                    dk2 += _dot_tn(dsb, qh)
                    dq2 += jnp.dot(dsb, jnp.where(msk, k2, zero), preferred_element_type=F32)
                    dcr_sc[h:h + 1, :] += -_colsum(dsc)
                dq_ref[qrows, lanes] += dq2 * ATT_SCALE
                dk_sc[:, lanes] += dk2
                dv_sc[:, lanes] += dv2

        pl.when(i > j)(functools.partial(tile, False))
        pl.when(i == j)(functools.partial(tile, True))

        @pl.when(i == nblk - 1)
        def _():
            dk_ref[...] = dk_sc[...]
            dv_ref[...] = dv_sc[...]
            dcr_ref[...] = dcr_sc[...]

    qblk = pl.BlockSpec((t, D_HALF), lambda j, i: (jnp.maximum(i, j), 0))
    qcol = pl.BlockSpec((N_HEADS, t, LANES), lambda j, i: (0, jnp.maximum(i, j), 0))
    kblk = pl.BlockSpec((t, D_HALF), lambda j, i: (j, 0))
    return pl.pallas_call(
        body, name="fox_attn_bwd", grid=(nblk, nblk),
        in_specs=[qblk, kblk, kblk, qblk, qcol, qcol, qcol, pl.BlockSpec((N_HEADS, t), lambda j, i: (0, j))],
        out_specs=[pl.BlockSpec((s, D_HALF), lambda j, i: (0, 0)), kblk, kblk,
                   pl.BlockSpec((N_HEADS, t), lambda j, i: (0, j))],
        out_shape=[jax.ShapeDtypeStruct((s, D_HALF), F32)] * 3 + [jax.ShapeDtypeStruct((N_HEADS, s), F32)],
        scratch_shapes=[pltpu.VMEM((t, D_HALF), F32), pltpu.VMEM((t, D_HALF), F32), pltpu.VMEM((N_HEADS, t), F32)],
        compiler_params=_params("arbitrary", "arbitrary"),
    )(q, k, v, do, lse, dd, cc, cr)


def _fox_prep_bwd(u_b, dq, dk, dv, dgate, dcum, fb, qg, kg, tm=256):
    s = u_b.shape[0]
    nb = s // tm

    def body(ub_ref, dq_ref, dk_ref, dv_ref, dg_ref, dc_ref, fb_ref, qg_ref, kg_ref,
             du_ref, dqg_ref, dkg_ref, dfb_ref, carry):
        i = pl.program_id(0)

        @pl.when(i == 0)
        def _():
            carry[...] = jnp.zeros_like(carry)
            dqg_ref[...] = jnp.zeros_like(dqg_ref)
            dkg_ref[...] = jnp.zeros_like(dkg_ref)
            dfb_ref[...] = jnp.zeros_like(dfb_ref)

        bd = _head_ones()
        for lo, g_ref, d_ref, dgain_ref in ((0, qg_ref, dq_ref, dqg_ref), (512, kg_ref, dk_ref, dkg_ref)):
            gain = g_ref[...]
            xh, rinv, _ = _head_rms(ub_ref[:, lo:lo + 512], gain, bd)
            dn = d_ref[...]
            dgain_ref[...] += _colsum(dn * xh)
            dxh = dn * gain
            du_ref[:, lo:lo + 512] = rinv * (dxh - xh * (_head_sum(dxh * xh, bd) * (1.0 / HEAD)))
        du_ref[:, 1024:1536] = dv_ref[...]
        du_ref[:, 1536:2048] = dg_ref[...]
        lane = lax.broadcasted_iota(jnp.int32, (1, LANES), 1)
        dc = dc_ref[...]
        dlogf = _exact_dot(dc, _tri(tm, False), ones_first=True) + carry[...]
        carry[...] += _colsum(dc)
        fl = ub_ref[:, 2048:2176] + fb_ref[...]
        dfl = jnp.where(lane < N_HEADS, dlogf * (1.0 - _sigmoid(fl)), 0.0)
        du_ref[:, 2048:2176] = dfl
        dfb_ref[...] += _colsum(dfl)

    rev = lambda w: pl.BlockSpec((tm, w), lambda i: (nb - 1 - i, 0))
    vec = lambda w: pl.BlockSpec((1, w), lambda i: (0, 0))
    return pl.pallas_call(
        body, name="fox_prep_bwd", grid=(nb,),
        in_specs=[rev(SEC)] + [rev(D_HALF)] * 4 + [rev(LANES), vec(LANES), vec(D_HALF), vec(D_HALF)],
        out_specs=[rev(SEC), vec(D_HALF), vec(D_HALF), vec(LANES)],
        out_shape=[jax.ShapeDtypeStruct((s, SEC), F32), jax.ShapeDtypeStruct((1, D_HALF), F32),
                   jax.ShapeDtypeStruct((1, D_HALF), F32), jax.ShapeDtypeStruct((1, LANES), F32)],
        scratch_shapes=[pltpu.VMEM((1, LANES), F32)],
        compiler_params=_params("arbitrary"),
    )(u_b, dq, dk, dv, dgate, dcum, fb, qg, kg)


def _merge(y, r, k, v, gate_a, o, u_b, h, x, tgt, w_g, wa, wb, wo, fg, lw, lb, rk, tm=256):
    s, d = x.shape

    def body(y_ref, r_ref, k_ref, v_ref, ga_ref, o_ref, gb_ref, h_ref, x_ref, t_ref, wg_ref, wa_ref, wb_ref, wo_ref,
             fg_ref, lw_ref, lb_ref, rk_ref,
             dx2_ref, dy_ref, drb_ref, dkb_ref, dvb_ref, dga_ref, do_ref, dgb_ref, dug_ref,
             dwa_ref, dwb_ref, dwo_ref, dfg_ref, loss_ref, dlw_ref, dlb_ref, drk_ref):
        i = pl.program_id(0)

        @pl.when(i == 0)
        def _():
            for ref in (dwa_ref, dwb_ref, dwo_ref, dfg_ref, loss_ref, dlw_ref, dlb_ref, drk_ref):
                ref[...] = jnp.zeros_like(ref)

        bd = _head_ones()
        wa_v, wb_v, wo_v, fg_v = wa_ref[...], wb_ref[...], wo_ref[...], fg_ref[...]
        rv, kv, vv, ga, lw_v, rk_v = r_ref[...], k_ref[...], v_ref[...], ga_ref[...], lw_ref[...], rk_ref[...]
        yn, rstd, rkk, sga, pre = _rwkv_post_math(y_ref[...], rv, kv, vv, ga, lw_v, lb_ref[...], rk_v, bd)
        silu_a = ga * sga
        gb, ov = gb_ref[...], o_ref[...]
        sgb = _sigmoid(gb)
        silu_b = gb * sgb
        ma = (pre * silu_a).astype(BF16)
        mb = (ov * silu_b).astype(BF16)
        ya = jnp.dot(ma, wa_v, preferred_element_type=F32)
        yb = jnp.dot(mb, wb_v, preferred_element_type=F32)
        ug = jnp.dot(h_ref[...], wg_ref[...], preferred_element_type=F32)
        sa = _sigmoid(ug[:, 0:d])
        sb = _sigmoid(ug[:, d:2 * d])
        merged = (sa * ya + sb * yb).astype(BF16)
        x2 = x_ref[...] + jnp.dot(merged, wo_v, preferred_element_type=F32)
        r2 = lax.rsqrt(jnp.mean(x2 * x2, axis=-1, keepdims=True) + RMS_EPS)
        x2h = x2 * r2
        err = x2h * fg_v - t_ref[...]
        loss_ref[...] += _colsum(err * err)
        dyo = err * (1.0 / d)
        dfg_ref[...] += _colsum(dyo * x2h)
        dx2h = dyo * fg_v
        dx2 = r2 * (dx2h - x2h * jnp.mean(dx2h * x2h, axis=-1, keepdims=True))
        dx2_ref[...] = dx2
        dx2b = dx2.astype(BF16)
        dmerged = _dot_nt(dx2b, wo_v)
        dwo_ref[...] += _dot_tn(merged, dx2b)
        dya = dmerged * sa
        dyb = dmerged * sb
        dug_ref[:, 0:d] = dya * ya * (1.0 - sa)
        dug_ref[:, d:2 * d] = dyb * yb * (1.0 - sb)
        dyab = dya.astype(BF16)
        dybb = dyb.astype(BF16)
        dwa_ref[...] += _dot_tn(ma, dyab)
        dwb_ref[...] += _dot_tn(mb, dybb)
        dmb = _dot_nt(dybb, wb_v)
        do_ref[...] = (dmb * silu_b).astype(BF16)
        dgb_ref[...] = dmb * ov * (sgb * (1.0 + gb * (1.0 - sgb)))
        dma = _dot_nt(dyab, wa_v)
        dga_ref[...] = dma * pre * (sga * (1.0 + ga * (1.0 - sga)))
        dpre = dma * silu_a
        dlw_ref[...] += _colsum(dpre * yn)
        dlb_ref[...] += _colsum(dpre)
        dyn = dpre * lw_v
        m1 = _head_sum(dyn, bd) * (1.0 / HEAD)
        m2 = _head_sum(dyn * yn, bd) * (1.0 / HEAD)
        dy_ref[...] = rstd * (dyn - m1 - yn * m2)
        dvb_ref[...] = dpre * rkk
        drkk = _head_sum(dpre * vv, bd)
        drb_ref[...] = drkk * kv * rk_v
        dkb_ref[...] = drkk * rv * rk_v
        drk_ref[...] += _colsum(drkk * rv * kv)

    row = lambda w: pl.BlockSpec((tm, w), lambda i: (i, 0))
    full = lambda a: pl.BlockSpec(a.shape, lambda i: (0, 0))
    once = lambda a: pl.BlockSpec(a.shape, lambda i: (0, 0), pipeline_mode=pl.Buffered(1))
    half = jax.ShapeDtypeStruct((s, D_HALF), F32)
    fshape = lambda a: jax.ShapeDtypeStruct(a.shape, F32)
    return pl.pallas_call(
        body, name="merge_fwd_bwd", grid=(s // tm,),
        in_specs=[row(D_HALF)] * 6 + [pl.BlockSpec((tm, D_HALF), lambda i: (i, 3)), row(d), row(d), row(d),
                                      once(w_g), once(wa), once(wb), once(wo), full(fg), full(lw), full(lb), full(rk)],
        out_specs=[row(d)] + [row(D_HALF)] * 7 + [row(GATE_COLS), full(wa), full(wb), full(wo), full(fg), full(fg),
                                                   full(lw), full(lb), full(rk)],
        out_shape=[jax.ShapeDtypeStruct((s, d), F32)] + [half] * 5 + [jax.ShapeDtypeStruct((s, D_HALF), BF16), half,
                                                                    jax.ShapeDtypeStruct((s, GATE_COLS), F32),
                                                                    fshape(wa), fshape(wb), fshape(wo), fshape(fg),
                                                                    fshape(fg), fshape(lw), fshape(lb), fshape(rk)],
        compiler_params=_params("arbitrary"),
    )(y, r, k, v, gate_a, o, u_b, h, x, tgt, w_g, wa, wb, wo, fg, lw, lb, rk)


def _lora_weight(w_up, a_up):
    z = jnp.zeros((LORA, D_HALF), w_up.dtype)
    return jnp.concatenate([jnp.concatenate([w_up, z], axis=1), jnp.concatenate([z, a_up], axis=1)], axis=0)


def _device_grads(x, tgt, p, w_a, w_up, a_up, late_weights, fwd_exchange=None, bwd_exchange=None, tail_exchange=None):
    wl = _lora_weight(w_up, a_up)
    rk = p["r_k"].reshape(1, D_HALF)
    fb = jnp.pad(p["f_bias"], ((0, 0), (0, LANES - N_HEADS)))
    qg = jnp.tile(p["q_norm_g"], (1, N_HEADS))
    kg = jnp.tile(p["k_norm_g"], (1, N_HEADS))
    fg = p["final_norm_g"].reshape(1, D_MODEL)
    mixer = (p["shift_mu"], wl, p["w0"], p["a0"], p["k_k"], p["k_a"])

    h = _rmsnorm_in(x, p["norm_g"])
    u_a = _matmul_nn(h, w_a, "inproj_rwkv")
    r, dec, k, v, av, bv, gate_a = _rwkv_prep(u_a, *mixer)
    y, st, arrived = _wkv_fwd(r, dec, k, av, bv, v, fwd_exchange)

    w_b, w_g, w_out_a, w_out_b, w_out = late_weights(arrived)
    u_b = _matmul_nn(h, w_b, "inproj_fox")
    q, kn, vb, cc, cr = _fox_prep(u_b, fb, qg, kg)
    o, lse = _attn_fwd(q, kn, vb, cc, cr)

    (dx2, dy, dr_b, dk_b, dv_b, dgate_a, do, dgate_b, du_g, dwa, dwb, dwo, dfg, loss_vec, dlw, dlb, drk) = _merge(
        y, r, k, v, gate_a, o, u_b, h, x, tgt, w_g, w_out_a, w_out_b, w_out, fg, p["lnx_w"], p["lnx_b"], rk)

    dd = _attn_bwd_rowdot(q, kn, vb, do, lse, cc, cr)
    dq, dk_att, dv_att, dcr = _attn_bwd(q, kn, vb, do, lse, dd, cc, cr)
    dcum = jnp.pad(dcr.T, ((0, 0), (0, LANES - N_HEADS)))
    du_b, dqg, dkg, dfb = _fox_prep_bwd(u_b, dq, dk_att, dv_att, dgate_b, dcum, fb, qg, kg)
    h_t = h.T
    dw_b = _matmul_tn_acc(h_t, du_b, "dw_fox")
    dw_g = _matmul_tn_acc(h_t, du_g, "dw_gate")

    scan_grads, sent = _wkv_bwd(r, dec, k, av, bv, v, dy, st,
                                bwd_exchange(dw_b, dw_g, dwa, dwb, dwo) if bwd_exchange else None)
    du_a, dmu, dwl, dw0, da0, dkkw, dkaw = _rwkv_prep_bwd(u_a, (*scan_grads, dr_b, dk_b, dv_b, dgate_a), *mixer)
    dw_a = _matmul_tn_acc(h_t, du_a, "dw_rwkv")
    dw_up, da_up = dwl[:LORA, :D_HALF], dwl[LORA:, D_HALF:]
    sent_last = _run_on_sequencer(tail_exchange(dw_a, dw_up, da_up), "scatter_tail", 1) if tail_exchange else []
    grad_x, dnorm_g, _ = _inproj_bwd(du_a, du_b, du_g, w_a, w_b, w_g, x, dx2, p["norm_g"])

    grads = dict(
        norm_g=dnorm_g, w_in=(dw_a, dw_b, dw_g), shift_mu=dmu,
        w_lora_up=dw_up, w0=dw0, a_lora_up=da_up, a0=da0, k_k=dkkw, k_a=dkaw,
        r_k=drk.reshape(1, N_HEADS, HEAD), lnx_w=dlw, lnx_b=dlb, f_bias=dfb[:, :N_HEADS],
        q_norm_g=dqg.reshape(N_HEADS, HEAD).sum(axis=0, keepdims=True),
        k_norm_g=dkg.reshape(N_HEADS, HEAD).sum(axis=0, keepdims=True),
        w_out_a=dwa, w_out_b=dwb, w_out=dwo, final_norm_g=dfg.reshape(D_MODEL))
    return loss_vec, grad_x, grads, sent, sent_last


CHIP_FLIPS = ((1, 0), (0, 1), (1, 1))
ANY = pl.BlockSpec(memory_space=pl.ANY)


def _position():
    return lax.axis_index("x"), lax.axis_index("y"), lax.axis_index("c")


def _flip(v, f):
    return 1 - v if f else v


def _both(a, b):
    if a is None:
        return b
    return a if b is None else jnp.logical_and(a, b)


def _when(cond, fn):
    if cond is None:
        fn()
    else:
        pl.when(cond)(fn)


class _Moves:
    def __init__(self, send_sems, recv_sems, local_sems):
        self.send_sems, self.recv_sems, self.local_sems = send_sems, recv_sems, local_sems
        self.remote, self.local = [], []

    def send(self, src, dst, peer, landing, send_if=None, recv_if=None):
        k = len(self.remote)
        sems = dict(send_sem=self.send_sems.at[k], recv_sem=self.recv_sems.at[k], device_id=peer, device_id_type=MESH)
        out = pltpu.make_async_remote_copy(src_ref=src, dst_ref=dst, **sems)
        arrival = pltpu.make_async_remote_copy(src_ref=src, dst_ref=landing, **sems)
        self.remote.append((out, arrival, send_if, recv_if))

    def copy(self, src, dst, cond=None):
        cp = pltpu.make_async_copy(src, dst, self.local_sems.at[len(self.local)])
        self.local.append((cp, cond))

    def start(self, also=None):
        for cp, cond in self.local:
            _when(_both(also, cond), cp.start)
        for out, _, send_if, _ in self.remote:
            _when(_both(also, send_if), out.start)

    def wait_arrivals(self, also=None):
        for _, arrival, _, recv_if in self.remote:
            _when(_both(also, recv_if), arrival.wait_recv)

    def wait_sent(self, also=None):
        for out, _, send_if, _ in self.remote:
            _when(_both(also, send_if), out.wait_send)
        for cp, cond in self.local:
            _when(_both(also, cond), cp.wait)

    def wait(self, also=None):
        self.wait_arrivals(also)
        self.wait_sent(also)


class _Exchange:
    def __init__(self, operands, out_shapes, n_remote, n_local, build, n_relay=0, relay=None):
        self.operands, self.out_shapes = list(operands), list(out_shapes)
        self.n_remote, self.n_local, self.build = n_remote, n_local, build
        self.n_relay, self.relay = n_relay, relay

    def scratch(self):
        return [pltpu.SemaphoreType.DMA((self.n_remote,)), pltpu.SemaphoreType.DMA((self.n_remote,)),
                pltpu.SemaphoreType.DMA((max(self.n_local, 1),))]

    def moves(self, in_refs, out_refs, sems):
        mv = _Moves(*sems)
        self.build(mv, in_refs, out_refs)
        return mv

    def run_alone(self, name):
        n_in, n_out = len(self.operands), len(self.out_shapes)
        relay_scratch = [pltpu.SemaphoreType.DMA((self.n_relay,))] * 2 if self.relay else []

        def body(*refs):
            ins, outs, sems = refs[:n_in], refs[n_in:n_in + n_out], refs[n_in + n_out:]
            mv = self.moves(ins, outs, sems[:3])
            mv.start()
            mv.wait_arrivals()
            if self.relay:
                passed = _Moves(sems[3], sems[4], None)
                self.relay(passed, ins, outs)
                passed.start()
                passed.wait()
            mv.wait_sent()

        return pl.pallas_call(
            body, name=name, in_specs=[ANY] * n_in, out_specs=[ANY] * n_out, out_shape=self.out_shapes,
            scratch_shapes=self.scratch() + relay_scratch, compiler_params=pltpu.CompilerParams(has_side_effects=True),
        )(*self.operands)


def _run_on_sequencer(exchange, name, collective_id):
    ins = [jax.new_ref(a, memory_space=pltpu.MemorySpace.HBM) for a in exchange.operands]
    outs = [jax.empty_ref(s, memory_space=pltpu.MemorySpace.HBM) for s in exchange.out_shapes]
    relay_scratch = [pltpu.SemaphoreType.DMA((exchange.n_relay,))] * 2 if exchange.relay else []

    def launch(*sems):
        x, y, c = _position()
        peers = [(_flip(x, fx), _flip(y, fy), c) for fx, fy in CHIP_FLIPS] + ([(x, y, 1 - c)] if exchange.relay else [])
        barrier = pltpu.get_barrier_semaphore()
        for peer in peers:
            pl.semaphore_signal(barrier, inc=1, device_id=peer, device_id_type=MESH)
        pl.semaphore_wait(barrier, len(peers))
        moves = exchange.moves(ins, outs, sems[:3])
        moves.start()
        moves.wait_arrivals()
        if exchange.relay:
            passed = _Moves(sems[3], sems[4], None)
            exchange.relay(passed, ins, outs)
            passed.start()
            passed.wait()
        moves.wait_sent()

    pl.kernel(launch, mesh=plsc.ScalarSubcoreMesh(axis_name="sequencer", num_cores=1), name=name,
              scratch_types=tuple(exchange.scratch() + relay_scratch),
              compiler_params=pltpu.CompilerParams(collective_id=collective_id))()
    return [o[...] for o in outs]


def _row_major_copy(a, name):
    r, c = a.shape
    tr = _row_tile(r)

    def body(a_ref, o_ref):
        o_ref[...] = a_ref[...]

    blk = pl.BlockSpec((tr, c), lambda i: (i, 0))
    return pl.pallas_call(body, name=name, grid=(r // tr,), in_specs=[blk], out_specs=blk,
                          out_shape=jax.ShapeDtypeStruct(a.shape, a.dtype), compiler_params=_params("parallel"))(a)


def _is_chip(x, y, chip):
    return jnp.logical_and(x == chip // 2, y == chip % 2)


def _gather_exchange(from_chip, from_all, split=()):
    n1, n2 = len(from_chip), len(from_all)

    def rows_of(t, c):
        half = from_chip[t][1].shape[0] // 2
        return pl.ds(c * half, half)

    def build(mv, ins, outs):
        x, y, c = _position()
        me = 2 * x + y
        for t, (chip, _) in enumerate(from_chip):
            mv.copy(ins[t], outs[t], cond=_is_chip(x, y, chip))
        for t in range(n2):
            mv.copy(ins[n1 + t], outs[n1 + t].at[me])
        for fx, fy in CHIP_FLIPS:
            px, py = _flip(x, fx), _flip(y, fy)
            peer = (px, py, c)
            for t, (chip, _) in enumerate(from_chip):
                part = rows_of(t, c) if t in split else slice(None)
                mv.send(ins[t].at[part], outs[t].at[part], peer, landing=outs[t].at[part],
                        send_if=_is_chip(x, y, chip), recv_if=_is_chip(px, py, chip))
            for t in range(n2):
                mv.send(ins[n1 + t], outs[n1 + t].at[me], peer, landing=outs[n1 + t].at[2 * px + py])

    def relay(mv, ins, outs):
        x, y, c = _position()
        for t in split:
            came = jnp.logical_not(_is_chip(x, y, from_chip[t][0]))
            mv.send(outs[t].at[rows_of(t, c)], outs[t].at[rows_of(t, c)], (x, y, 1 - c),
                    landing=outs[t].at[rows_of(t, 1 - c)], send_if=came, recv_if=came)

    arrays = [a for _, a in from_chip] + list(from_all)
    shapes = [jax.ShapeDtypeStruct(a.shape, a.dtype) for _, a in from_chip]
    shapes += [jax.ShapeDtypeStruct((N_CHIPS,) + a.shape, a.dtype) for a in from_all]
    return _Exchange(arrays, shapes, len(CHIP_FLIPS) * (n1 + n2), n1 + n2, build,
                     n_relay=len(split), relay=relay if split else None)


def _scatter_exchange(to_chip, to_all):
    n1, n2 = len(to_chip), len(to_all)

    def build(mv, ins, outs):
        x, y, c = _position()
        for f, (fx, fy) in enumerate(CHIP_FLIPS):
            px, py = _flip(x, fx), _flip(y, fy)
            peer = (px, py, c)
            for t, (chip, _) in enumerate(to_chip):
                mv.send(ins[t], outs[t].at[f], peer, landing=outs[t].at[f],
                        send_if=_is_chip(px, py, chip), recv_if=_is_chip(x, y, chip))
            for t in range(n2):
                mv.send(ins[n1 + t].at[2 * px + py], outs[n1 + t].at[f], peer, landing=outs[n1 + t].at[f])

    arrays = [a for _, a in to_chip] + list(to_all)
    shapes = [jax.ShapeDtypeStruct((len(CHIP_FLIPS),) + a.shape, a.dtype) for _, a in to_chip]
    shapes += [jax.ShapeDtypeStruct((len(CHIP_FLIPS),) + a.shape[1:], a.dtype) for a in to_all]
    return _Exchange(arrays, shapes, len(CHIP_FLIPS) * (n1 + n2), 0, build)


def _swap_sibling(tensors, name):
    n = len(tensors)

    def body(*refs):
        ins, outs = refs[:n], refs[n:2 * n]
        send_sems, recv_sems = refs[2 * n:]
        x, y, c = _position()
        copies = [pltpu.make_async_remote_copy(
            src_ref=ins[t], dst_ref=outs[t], send_sem=send_sems.at[t], recv_sem=recv_sems.at[t],
            device_id=(x, y, 1 - c), device_id_type=MESH) for t in range(n)]
        for cp in copies:
            cp.start()
        for cp in copies:
            cp.wait_recv()
        for cp in copies:
            cp.wait_send()

    return pl.pallas_call(
        body, name=name, in_specs=[ANY] * n, out_specs=[ANY] * n,
        out_shape=[jax.ShapeDtypeStruct(a.shape, a.dtype) for a in tensors],
        scratch_shapes=[pltpu.SemaphoreType.DMA((n,)), pltpu.SemaphoreType.DMA((n,))],
        compiler_params=pltpu.CompilerParams(has_side_effects=True),
    )(*tensors)


def _allreduce_small(slab):
    stages = 3

    def body(x_ref, o_ref, buf, send_sems, recv_sems):
        x, y, c = _position()
        peers = ((1 - x, y, c), (x, 1 - y, c), (x, y, 1 - c))
        o_ref[...] = x_ref[...]
        for k, peer in enumerate(peers):
            cp = pltpu.make_async_remote_copy(src_ref=o_ref, dst_ref=buf.at[k], send_sem=send_sems.at[k],
                                              recv_sem=recv_sems.at[k], device_id=peer, device_id_type=MESH)
            cp.start()
            cp.wait()
            o_ref[...] = o_ref[...] + buf[k]

    return pl.pallas_call(
        body, name="allreduce_small",
        in_specs=[pl.BlockSpec(memory_space=pltpu.VMEM)], out_specs=pl.BlockSpec(memory_space=pltpu.VMEM),
        out_shape=jax.ShapeDtypeStruct(slab.shape, slab.dtype),
        scratch_shapes=[pltpu.VMEM((stages,) + slab.shape, slab.dtype),
                        pltpu.SemaphoreType.DMA((stages,)), pltpu.SemaphoreType.DMA((stages,))],
        compiler_params=pltpu.CompilerParams(has_side_effects=True),
    )(slab)


def _row_tile(r):
    return min(r, 256)


def _sum4(stack, recv, me):
    _, r, c = stack.shape
    tr = _row_tile(r)

    def body(me_ref, own_ref, recv_ref, o_ref):
        o_ref[...] = (((own_ref[...] + recv_ref[0].astype(F32)) + recv_ref[1].astype(F32))
                      + recv_ref[2].astype(F32))

    return pl.pallas_call(
        body, name="sum_partials",
        grid_spec=pltpu.PrefetchScalarGridSpec(
            num_scalar_prefetch=1, grid=(r // tr,),
            in_specs=[pl.BlockSpec((None, tr, c), lambda i, me_ref: (me_ref[0], i, 0)),
                      pl.BlockSpec((len(CHIP_FLIPS), tr, c), lambda i, me_ref: (0, i, 0))],
            out_specs=pl.BlockSpec((tr, c), lambda i, me_ref: (i, 0))),
        out_shape=jax.ShapeDtypeStruct((r, c), F32), compiler_params=_params("parallel"),
    )(me, stack, recv)


def _sum_block(own, recv):
    r, c = own.shape
    tr = _row_tile(r)

    def body(own_ref, recv_ref, o_ref):
        o_ref[...] = (((own_ref[...] + recv_ref[0].astype(F32)) + recv_ref[1].astype(F32))
                      + recv_ref[2].astype(F32))

    return pl.pallas_call(
        body, name="sum_block", grid=(r // tr,),
        in_specs=[pl.BlockSpec((tr, c), lambda i: (i, 0)), pl.BlockSpec((len(CHIP_FLIPS), tr, c), lambda i: (0, i, 0))],
        out_specs=pl.BlockSpec((tr, c), lambda i: (i, 0)),
        out_shape=jax.ShapeDtypeStruct((r, c), F32), compiler_params=_params("parallel"),
    )(own, recv)


def _adamw_math(w, g, m, v):
    m = ADAM_B1 * m + (1.0 - ADAM_B1) * g
    v = ADAM_B2 * v + (1.0 - ADAM_B2) * (g * g)
    m_hat = m / (1.0 - ADAM_B1 ** ADAM_STEP)
    v_hat = v / (1.0 - ADAM_B2 ** ADAM_STEP)
    delta = -ADAM_LR * (m_hat / (jnp.sqrt(v_hat) + ADAM_EPS) + ADAM_WD * w)
    return delta, m, v


def _adamw(w, m, v, g_parts, name):
    r, c = w.shape
    tr = _row_tile(r)
    n = len(g_parts)

    def body(*refs):
        w_ref, m_ref, v_ref = refs[:3]
        g_refs = refs[3:3 + n]
        g_out, d_out, m_out, v_out = refs[3 + n:]
        g = g_refs[0][...]
        for ref in g_refs[1:]:
            g = g + ref[...]
        g_out[...] = g
        d_out[...], m_out[...], v_out[...] = _adamw_math(w_ref[...], g, m_ref[...], v_ref[...])

    blk = pl.BlockSpec((tr, c), lambda i: (i, 0))
    return pl.pallas_call(
        body, name=name, grid=(r // tr,), in_specs=[blk] * (3 + n), out_specs=[blk] * 4,
        out_shape=[jax.ShapeDtypeStruct((r, c), F32)] * 4, compiler_params=_params("parallel"),
    )(w, m, v, *g_parts)


def _adamw_small(total, w, m, v):
    sizes = [w[n].size for n in SMALL]
    flat = lambda d: [d[n].reshape(1, -1) for n in SMALL]
    k = len(SMALL)

    def body(*refs):
        total_ref, w_refs, m_refs, v_refs = refs[0], refs[1:1 + k], refs[1 + k:1 + 2 * k], refs[1 + 2 * k:1 + 3 * k]
        outs = refs[1 + 3 * k:]
        for i, size in enumerate(sizes):
            g = total_ref[i:i + 1, 0:size]
            outs[i][...] = g
            outs[k + i][...], outs[2 * k + i][...], outs[3 * k + i][...] = _adamw_math(
                w_refs[i][...], g, m_refs[i][...], v_refs[i][...])

    res = pl.pallas_call(
        body, name="adamw_small", out_shape=[jax.ShapeDtypeStruct((1, size), F32) for size in sizes] * 4,
        compiler_params=_params(),
    )(total, *flat(w), *flat(m), *flat(v))
    return [{n: res[j * k + i].reshape(w[n].shape) for i, n in enumerate(SMALL)} for j in range(4)]


SHARDED = ("w_in", "w_lora_up", "a_lora_up", "w_out_a", "w_out_b", "w_out")
ROW_SHARDED = ("w_out",)
SMALL = ("norm_g", "shift_mu", "w0", "a0", "k_k", "k_a", "r_k", "lnx_w", "lnx_b", "f_bias", "q_norm_g", "k_norm_g",
         "final_norm_g")
WEIGHTS = ("norm_g", "w_in", "shift_mu", "w_lora_up", "w0", "a_lora_up", "a0", "k_k", "k_a", "r_k", "lnx_w", "lnx_b",
           "f_bias", "q_norm_g", "k_norm_g", "w_out_a", "w_out_b", "w_out", "final_norm_g")
SLAB_ROWS = 16
SLAB_COLS = SEC


def _to_slab(named, extra=None):
    rows = [jnp.pad(named[n].reshape(1, -1), ((0, 0), (0, SLAB_COLS - named[n].size))) for n in SMALL]
    if extra is not None:
        rows.append(jnp.pad(extra.reshape(1, -1), ((0, 0), (0, SLAB_COLS - extra.size))))
    rows.append(jnp.zeros((SLAB_ROWS - len(rows), SLAB_COLS), F32))
    return jnp.concatenate(rows, axis=0)


def _by_chip(g, name):
    if name in ROW_SHARDED:
        return g.reshape(N_CHIPS, g.shape[0] // N_CHIPS, g.shape[1])
    r, c = g.shape
    return g.reshape(r, N_CHIPS, c // N_CHIPS).transpose(1, 0, 2)


def _from_chips(stack, name):
    if name in ROW_SHARDED:
        return stack.reshape(-1, stack.shape[2])
    _, r, c = stack.shape
    return stack.transpose(1, 0, 2).reshape(r, N_CHIPS * c)


def kernel(x, norm_g, w_in, shift_mu, w_lora_up, w0, a_lora_up, a0, k_k, k_a, r_k, lnx_w, lnx_b, f_bias, q_norm_g, k_norm_g, w_out_a, w_out_b, w_out, final_norm_g, loss_target, m_norm_g, m_w_in, m_shift_mu, m_w_lora_up, m_w0, m_a_lora_up, m_a0, m_k_k, m_k_a, m_r_k, m_lnx_w, m_lnx_b, m_f_bias, m_q_norm_g, m_k_norm_g, m_w_out_a, m_w_out_b, m_w_out, m_final_norm_g, v_norm_g, v_w_in, v_shift_mu, v_w_lora_up, v_w0, v_a_lora_up, v_a0, v_k_k, v_k_a, v_r_k, v_lnx_w, v_lnx_b, v_f_bias, v_q_norm_g, v_k_norm_g, v_w_out_a, v_w_out_b, v_w_out, v_final_norm_g):
    w = dict(norm_g=norm_g, w_in=w_in, shift_mu=shift_mu, w_lora_up=w_lora_up, w0=w0, a_lora_up=a_lora_up, a0=a0,
             k_k=k_k, k_a=k_a, r_k=r_k, lnx_w=lnx_w, lnx_b=lnx_b, f_bias=f_bias, q_norm_g=q_norm_g,
             k_norm_g=k_norm_g, w_out_a=w_out_a, w_out_b=w_out_b, w_out=w_out, final_norm_g=final_norm_g)
    m = dict(norm_g=m_norm_g, w_in=m_w_in, shift_mu=m_shift_mu, w_lora_up=m_w_lora_up, w0=m_w0,
             a_lora_up=m_a_lora_up, a0=m_a0, k_k=m_k_k, k_a=m_k_a, r_k=m_r_k, lnx_w=m_lnx_w, lnx_b=m_lnx_b,
             f_bias=m_f_bias, q_norm_g=m_q_norm_g, k_norm_g=m_k_norm_g, w_out_a=m_w_out_a, w_out_b=m_w_out_b,
             w_out=m_w_out, final_norm_g=m_final_norm_g)
    v = dict(norm_g=v_norm_g, w_in=v_w_in, shift_mu=v_shift_mu, w_lora_up=v_w_lora_up, w0=v_w0,
             a_lora_up=v_a_lora_up, a0=v_a0, k_k=v_k_k, k_a=v_k_a, r_k=v_r_k, lnx_w=v_lnx_w, lnx_b=v_lnx_b,
             f_bias=v_f_bias, q_norm_g=v_q_norm_g, k_norm_g=v_k_norm_g, w_out_a=v_w_out_a, w_out_b=v_w_out_b,
             w_out=v_w_out, final_norm_g=v_final_norm_g)
    shapes = {n: w[n].shape for n in WEIGHTS}

    shard = {n: w[n][0].astype(BF16) for n in SHARDED}
    late = ("w_out_a", "w_out_b", "w_out")
    loras = ("w_lora_up", "a_lora_up")
    w_in_head, w_in_tail = shard["w_in"][:, :A_TAIL], shard["w_in"][:, A_TAIL:]
    shard0, shard1_head, up_stack, aup_stack = _run_on_sequencer(_gather_exchange(
        [(0, shard["w_in"]), (1, w_in_head)], [shard[n] for n in loras], split=(0,)), "gather_early", 2)
    moments = (_row_major_copy(m["w_in"][0], "m_w_in_rows"), _row_major_copy(v["w_in"][0], "v_w_in_rows"))
    shard0, moments = lax.optimization_barrier((shard0, moments))
    w_a = jnp.concatenate([shard0, shard1_head], axis=1)

    def late_weights(arrived):
        shard1_tail, shard2, shard3 = arrived[:3]
        w_b = jnp.concatenate([shard1_tail, shard2[:, :B_TAIL], jnp.zeros((D_MODEL, SEC - FOX_REAL), BF16)], axis=1)
        w_g = jnp.concatenate([shard2[:, B_TAIL:], shard3], axis=1)
        return (w_b, w_g, *[_from_chips(s, n) for n, s in zip(late, arrived[3:])])

    own = {}

    def bwd_exchange(dw_b, dw_g, dwa, dwb, dwo):
        own["tail1"] = dw_b[:, :B_HEAD]
        own["block2"] = jnp.concatenate([dw_b[:, B_HEAD:FOX_REAL], dw_g[:, :G_HEAD]], axis=1)
        own["block3"] = dw_g[:, G_HEAD:]
        own.update({n: _by_chip(g, n) for n, g in zip(late, (dwa, dwb, dwo))})
        return _scatter_exchange([(1, own["tail1"].astype(BF16)), (2, own["block2"].astype(BF16)),
                                  (3, own["block3"].astype(BF16))], [own[n].astype(BF16) for n in late])

    def tail_exchange(dw_a, dw_up, da_up):
        own["block0"], own["head1"] = dw_a[:, :SHARD_COLS], dw_a[:, SHARD_COLS:]
        own.update({n: _by_chip(g, n) for n, g in zip(loras, (dw_up, da_up))})
        return _scatter_exchange([(0, own["block0"].astype(BF16)), (1, own["head1"].astype(BF16))],
                                 [own[n].astype(BF16) for n in loras])

    small = {n: w[n] for n in SMALL}
    loss_vec, grad_x, grads, sent, sent_last = _device_grads(
        x[0], loss_target[0], small, w_a, _from_chips(up_stack, "w_lora_up"), _from_chips(aup_stack, "a_lora_up"),
        late_weights, _gather_exchange([(1, w_in_tail), (2, shard["w_in"]), (3, shard["w_in"])], [shard[n] for n in late]),
        bwd_exchange, tail_exchange)

    total = _allreduce_small(_to_slab(grads, extra=loss_vec))
    loss = (0.5 / D_MODEL) * jnp.sum(total[len(SMALL)])
    out_g, out_d, out_m, out_v = _adamw_small(total, w, m, v)

    xpos, ypos, _ = _position()
    me = (2 * xpos + ypos).astype(jnp.int32).reshape(1)
    core_sum = {n: _sum4(own[n], r, me) for n, r in zip(late, sent[3:])}
    theirs = dict(zip(late, _swap_sibling([core_sum[n] for n in late], "swap_sibling_early")))
    sent_last, out_d["norm_g"], theirs = lax.optimization_barrier((sent_last, out_d["norm_g"], theirs))
    core_sum["w_in"] = lax.switch(me[0], [
        lambda: _sum_block(own["block0"], sent_last[0]),
        lambda: jnp.concatenate([_sum_block(own["head1"], sent_last[1]), _sum_block(own["tail1"], sent[0])], axis=1),
        lambda: _sum_block(own["block2"], sent[1]),
        lambda: _sum_block(own["block3"], sent[2])])
    core_sum.update({n: _sum4(own[n], r, me) for n, r in zip(loras, sent_last[2:])})
    rest = ("w_in",) + loras
    theirs.update(zip(rest, _swap_sibling([core_sum[n] for n in rest], "swap_sibling")))
    for n in SHARDED:
        m_n, v_n = moments if n == "w_in" else (m[n][0], v[n][0])
        g, d, m2, v2 = _adamw(w[n][0], m_n, v_n, [core_sum[n], theirs[n]], "adamw_" + n)
        out_g[n], out_d[n], out_m[n], out_v[n] = (a.reshape(shapes[n]) for a in (g, d, m2, v2))

    return (loss, grad_x.reshape(x.shape), *[out_g[n] for n in WEIGHTS], *[out_d[n] for n in WEIGHTS],
            *[out_m[n] for n in WEIGHTS], *[out_v[n] for n in WEIGHTS])
```

```python
import functools
import math

import jax
import jax.numpy as jnp
from jax import lax
from jax.experimental import pallas as pl
from jax.experimental.pallas import tpu as pltpu
from jax.experimental.pallas import tpu_sc as plsc

F32 = jnp.float32
BF16 = jnp.bfloat16

D_MODEL = 1024
D_HALF = 512
HEAD = 64
N_HEADS = 8
LORA = 64
RWKV_COLS = 2176
FOX_REAL = 2056
SEC = 2176
GATE_COLS = 2048
IN_COLS = 6280
N_CHIPS = 4
SHARD_COLS = IN_COLS // N_CHIPS
A_TAIL = RWKV_COLS - SHARD_COLS
B_HEAD = SHARD_COLS - A_TAIL
B_TAIL = FOX_REAL - B_HEAD
G_HEAD = SHARD_COLS - B_TAIL
RMS_EPS = 1e-6
LNX_EPS = 64e-5
ATT_SCALE = HEAD ** -0.5
NEG = -1e30

ADAM_LR = 0.001
ADAM_B1 = 0.9
ADAM_B2 = 0.999
ADAM_EPS = 1e-08
ADAM_WD = 0.01
ADAM_STEP = 10

LANES = 128
SUBLANES = 8
VMEM_LIMIT = 56 * 1024 * 1024
MESH = pl.DeviceIdType.MESH


def _params(*sem):
    return pltpu.CompilerParams(dimension_semantics=sem if sem else None, vmem_limit_bytes=VMEM_LIMIT)


def _sigmoid(x):
    return 1.0 / (1.0 + jnp.exp(-x))


def _log_sigmoid(x):
    return jnp.minimum(x, 0.0) - jnp.log(1.0 + jnp.exp(-jnp.abs(x)))


def _head_ones():
    r = lax.broadcasted_iota(jnp.int32, (LANES, LANES), 0) >> 6
    c = lax.broadcasted_iota(jnp.int32, (LANES, LANES), 1) >> 6
    return (r == c).astype(BF16)


def _split3(x):
    hi = x.astype(BF16)
    r1 = x - hi.astype(F32)
    mid = r1.astype(BF16)
    lo = (r1 - mid.astype(F32)).astype(BF16)
    return hi, mid, lo


def _exact_dot(x, ones_bf16, ones_first=False):
    out = None
    for piece in _split3(x):
        if ones_first:
            t = jnp.dot(ones_bf16, piece, preferred_element_type=F32)
        else:
            t = jnp.dot(piece, ones_bf16, preferred_element_type=F32)
        out = t if out is None else out + t
    return out


def _head_sum(x, bd):
    n = x.shape[1] // LANES
    parts = [_exact_dot(x[:, i * LANES:(i + 1) * LANES], bd) for i in range(n)]
    return parts[0] if n == 1 else jnp.concatenate(parts, axis=1)


def _dot_nt(a, b):
    return lax.dot_general(a, b, (((1,), (1,)), ((), ())), preferred_element_type=F32)


def _dot_tn(a, b):
    return lax.dot_general(a, b, (((0,), (0,)), ((), ())), preferred_element_type=F32)


def _colsum(x):
    return jnp.sum(x, axis=0, keepdims=True)


def _rmsnorm_in(x, g, tm=512):
    s, d = x.shape

    def body(x_ref, g_ref, h_ref):
        xv = x_ref[...]
        r = lax.rsqrt(jnp.mean(xv * xv, axis=-1, keepdims=True) + RMS_EPS)
        h_ref[...] = (xv * r * g_ref[...]).astype(BF16)

    return pl.pallas_call(
        body, name="rmsnorm_in", grid=(s // tm,),
        in_specs=[pl.BlockSpec((tm, d), lambda i: (i, 0)), pl.BlockSpec((1, d), lambda i: (0, 0))],
        out_specs=pl.BlockSpec((tm, d), lambda i: (i, 0)),
        out_shape=jax.ShapeDtypeStruct((s, d), BF16), compiler_params=_params("parallel"),
    )(x, g)


def _matmul_nn(a, b, name, tm=512):
    m, k = a.shape
    n = b.shape[1]

    def body(a_ref, b_ref, o_ref):
        o_ref[...] = jnp.dot(a_ref[...], b_ref[...], preferred_element_type=F32)

    return pl.pallas_call(
        body, name=name, grid=(m // tm,),
        in_specs=[pl.BlockSpec((tm, k), lambda i: (i, 0)), pl.BlockSpec((k, n), lambda i: (0, 0))],
        out_specs=pl.BlockSpec((tm, n), lambda i: (i, 0)),
        out_shape=jax.ShapeDtypeStruct((m, n), F32), compiler_params=_params("parallel"),
    )(a, b)


def _matmul_tn_acc(at, b, name, tk=512):
    m, k = at.shape
    n = b.shape[1]

    def body(a_ref, b_ref, o_ref):
        j = pl.program_id(0)

        @pl.when(j == 0)
        def _():
            o_ref[...] = jnp.zeros_like(o_ref)

        o_ref[...] += jnp.dot(a_ref[...], b_ref[...].astype(BF16), preferred_element_type=F32)

    return pl.pallas_call(
        body, name=name, grid=(k // tk,),
        in_specs=[pl.BlockSpec((m, tk), lambda j: (0, j)), pl.BlockSpec((tk, n), lambda j: (j, 0))],
        out_specs=pl.BlockSpec((m, n), lambda j: (0, 0)),
        out_shape=jax.ShapeDtypeStruct((m, n), F32), compiler_params=_params("arbitrary"),
    )(at, b)


def _inproj_bwd(du_a, du_b, du_g, w_a, w_b, w_g, x, dx2, g, exchange=None, tm=256):
    s, d = x.shape
    nb = s // tm

    def body(*refs):
        ((da_ref, db_ref, dg_ref, wa_ref, wb_ref, wg_ref, x_ref, dx2_ref, g_ref), (gx_ref, gg_ref), _,
         moves) = _split_refs(refs, 9, 2, exchange)
        i = pl.program_id(0)
        if moves:
            moves.start(also=(i == 0))

        @pl.when(i == 0)
        def _():
            gg_ref[...] = jnp.zeros_like(gg_ref)

        dh = _dot_nt(da_ref[...].astype(BF16), wa_ref[...])
        dh += _dot_nt(db_ref[...].astype(BF16), wb_ref[...])
        dh += _dot_nt(dg_ref[...].astype(BF16), wg_ref[...])
        xv = x_ref[...]
        r = lax.rsqrt(jnp.mean(xv * xv, axis=-1, keepdims=True) + RMS_EPS)
        xh = xv * r
        gg_ref[...] += _colsum(dh * xh)
        dxh = dh * g_ref[...]
        gx_ref[...] = dx2_ref[...] + r * (dxh - xh * jnp.mean(dxh * xh, axis=-1, keepdims=True))
        if moves:
            moves.wait(also=(i == nb - 1))

    row = lambda w: pl.BlockSpec((tm, w), lambda i: (i, 0))
    full = lambda a: pl.BlockSpec(a.shape, lambda i: (0, 0))
    ex_in = exchange.operands if exchange else []
    ex_out = exchange.out_shapes if exchange else []
    res = pl.pallas_call(
        body, name="inproj_bwd", grid=(nb,),
        in_specs=[row(SEC), row(SEC), row(GATE_COLS), full(w_a), full(w_b), full(w_g), row(d), row(d), full(g)]
                 + [ANY] * len(ex_in),
        out_specs=[row(d), pl.BlockSpec((1, d), lambda i: (0, 0))] + [ANY] * len(ex_out),
        out_shape=[jax.ShapeDtypeStruct((s, d), F32), jax.ShapeDtypeStruct((1, d), F32)] + ex_out,
        scratch_shapes=exchange.scratch() if exchange else [],
        compiler_params=_params("arbitrary"),
    )(du_a, du_b, du_g, w_a, w_b, w_g, x, dx2, g, *ex_in)
    return res[0], res[1], list(res[2:])


def _rwkv_elementwise(ua, prev_row, first, mu, wl, w0, a0, kkw, kaw, bd):
    tm = ua.shape[0]
    rows = lax.broadcasted_iota(jnp.int32, (tm, 1), 0)
    prev = jnp.where(first, jnp.zeros_like(prev_row), prev_row)
    shifted = jnp.where(rows == 0, prev, pltpu.roll(ua, 1, 0))
    delta = shifted - ua
    us = ua + delta * mu
    r = us[:, 0:512]
    k0 = us[:, 512:1024]
    v = us[:, 1024:1536]
    lo = us[:, 1536:1664]
    gate = us[:, 1664:2176]
    lane = lax.broadcasted_iota(jnp.int32, (1, LANES), 1)
    th = jnp.tanh(lo)
    lin = jnp.where(lane < LORA, th, lo)
    ll = jnp.dot(lin.astype(BF16), wl, preferred_element_type=F32)
    sz = _sigmoid(w0 + ll[:, :512])
    e = sz * math.exp(-0.5)
    dec = jnp.exp(-e)
    a = _sigmoid(a0 + ll[:, 512:])
    kk0 = k0 * kkw
    ss = _head_sum(kk0 * kk0, bd)
    nrm = jnp.maximum(jnp.sqrt(ss), 1e-12)
    kk = kk0 / nrm
    k = k0 * (1.0 + (a - 1.0) * kaw)
    return dict(delta=delta, us=us, r=r, k0=k0, v=v, lo=lo, gate=gate, th=th, lin=lin, sz=sz, e=e, dec=dec,
                a=a, kk0=kk0, ss=ss, nrm=nrm, kk=kk, k=k)


def _rwkv_prep(u_a, mu, wl, w0, a0, kkw, kaw, tm=256):
    s = u_a.shape[0]

    def body(ua_ref, prev_ref, mu_ref, wl_ref, w0_ref, a0_ref, kkw_ref, kaw_ref,
             r_ref, w_ref, k_ref, v_ref, a_ref, b_ref, g_ref):
        i = pl.program_id(0)
        f = _rwkv_elementwise(ua_ref[...], prev_ref[7:8, :], i == 0, mu_ref[...], wl_ref[...], w0_ref[...],
                              a0_ref[...], kkw_ref[...], kaw_ref[...], _head_ones())
        r_ref[...] = f["r"]
        w_ref[...] = f["dec"]
        k_ref[...] = f["k"]
        v_ref[...] = f["v"]
        a_ref[...] = -f["kk"]
        b_ref[...] = f["kk"] * f["a"]
        g_ref[...] = f["gate"]

    vec = lambda w: pl.BlockSpec((1, w), lambda i: (0, 0))
    out = pl.BlockSpec((tm, D_HALF), lambda i: (i, 0))
    return pl.pallas_call(
        body, name="rwkv_prep", grid=(s // tm,),
        in_specs=[pl.BlockSpec((tm, SEC), lambda i: (i, 0)),
                  pl.BlockSpec((8, SEC), lambda i: (jnp.maximum(i * (tm // 8) - 1, 0), 0)),
                  vec(SEC), pl.BlockSpec((LANES, 2 * D_HALF), lambda i: (0, 0)),
                  vec(D_HALF), vec(D_HALF), vec(D_HALF), vec(D_HALF)],
        out_specs=[out] * 7,
        out_shape=[jax.ShapeDtypeStruct((s, D_HALF), F32)] * 7,
        compiler_params=_params("parallel"),
    )(u_a, u_a, mu, wl, w0, a0, kkw, kaw)


SCAN_TB = 128
N_PAIRS = 4


def _pair_sum(x, left):
    s_l = jnp.sum(jnp.where(left, x, 0.0), axis=1, keepdims=True)
    s_r = jnp.sum(jnp.where(left, 0.0, x), axis=1, keepdims=True)
    return jnp.where(left, s_l, s_r)


def _pair_dot(x, row_l, row_r, left):
    s_l = jnp.sum(x * row_l, axis=1, keepdims=True)
    s_r = jnp.sum(x * row_r, axis=1, keepdims=True)
    return jnp.where(left, s_l, s_r)


def _halves(rows8):
    lane = lax.broadcasted_iota(jnp.int32, rows8.shape, 1)
    keep_left = (lane & (LANES - 1)) < HEAD
    return jnp.where(keep_left, rows8, 0.0), jnp.where(keep_left, 0.0, rows8)


def _quad_consts():
    lane = lax.broadcasted_iota(jnp.int32, (HEAD, 2 * LANES), 1)
    rowi = lax.broadcasted_iota(jnp.int32, (HEAD, 2 * LANES), 0)
    diag2 = rowi == (lane & (HEAD - 1))
    r = lax.broadcasted_iota(jnp.int32, (2 * LANES, 2 * LANES), 0) >> 6
    c = lax.broadcasted_iota(jnp.int32, (2 * LANES, 2 * LANES), 1) >> 6
    return diag2, (r == c).astype(BF16)


def _rows_to_columns(x8, diag2, bd2):
    lhs = jnp.concatenate([jnp.where(diag2, x8[i:i + 1], 0.0).astype(BF16) for i in range(SUBLANES)], axis=0)
    return jnp.dot(lhs, bd2, preferred_element_type=F32)


def _diag_rows(qtile, diag2, bd2, sub_row2):
    res = jnp.dot(qtile, bd2, preferred_element_type=F32)
    out = jnp.zeros((SUBLANES, 2 * LANES), F32)
    for i in range(SUBLANES):
        out = jnp.where(sub_row2 == i, _colsum(jnp.where(diag2, res[i * HEAD:(i + 1) * HEAD], 0.0)), out)
    return out


def _store_tile(qbuf, slot, p, i, x):
    qbuf[slot, p // 2, i * HEAD:(i + 1) * HEAD, (p % 2) * LANES:(p % 2 + 1) * LANES] = x.astype(BF16)


def _left_half():
    return lax.broadcasted_iota(jnp.int32, (HEAD, LANES), 1) < HEAD


def _split_refs(refs, n_rows, n_out, exchange):
    n_in = len(exchange.operands) if exchange else 0
    n_ex_out = len(exchange.out_shapes) if exchange else 0
    refs = list(refs)
    rows, refs = refs[:n_rows], refs[n_rows:]
    ex_in, refs = refs[:n_in], refs[n_in:]
    outs, refs = refs[:n_out], refs[n_out:]
    ex_out, refs = refs[:n_ex_out], refs[n_ex_out:]
    scratch, sems = (refs[:-3], refs[-3:]) if exchange else (refs, None)
    moves = exchange.moves(ex_in, ex_out, sems) if exchange else None
    return rows, outs, scratch, moves


def _wkv_fwd(r, w, k, a, b, v, exchange=None):
    s = r.shape[0]
    tb = SCAN_TB
    nb = s // tb

    def body(*refs):
        (r_ref, w_ref, k_ref, a_ref, b_ref, v_ref), (y_ref, st_ref), (state, vbuf, qbuf), moves = _split_refs(
            refs, 6, 2, exchange)
        g = pl.program_id(0)
        if moves:
            moves.start(also=(g == 0))

        @pl.when(g == 0)
        def _():
            state[...] = jnp.zeros_like(state)
            qbuf[...] = jnp.zeros_like(qbuf)

        left = _left_half()
        diag2, bd2 = _quad_consts()
        sub_row2 = lax.broadcasted_iota(jnp.int32, (SUBLANES, 2 * LANES), 0)
        groups = tb // SUBLANES
        quads = [slice(g2 * 2 * LANES, (g2 + 1) * 2 * LANES) for g2 in range(2)]

        def rows_of(q):
            return pl.ds(pl.multiple_of(q * SUBLANES, SUBLANES), SUBLANES)

        def v_tiles(q, slot):
            v8 = v_ref[rows_of(q), :]
            for g2 in range(2):
                vbuf[slot, g2] = _rows_to_columns(v8[:, quads[g2]], diag2, bd2)

        def chain(q, slot):
            rows8 = rows_of(q)
            a8, w8, b8, k8, r8 = (x[rows8, :] for x in (a_ref, w_ref, b_ref, k_ref, r_ref))
            pairs = [slice(p * LANES, (p + 1) * LANES) for p in range(N_PAIRS)]
            a_next = pltpu.roll(a8, SUBLANES - 1, 0)
            (a8_l, a8_r), (wa8_l, wa8_r) = _halves(a8), _halves(w8 * a_next)
            ba8 =jnp.concatenate([_pair_sum(b8[:, pr] * a_next[:, pr], left[0:SUBLANES]) for pr in pairs], axis=1)
            ka8 = jnp.concatenate([_pair_sum(k8[:, pr] * a_next[:, pr], left[0:SUBLANES]) for pr in pairs], axis=1)
            sp = [state[p] for p in range(N_PAIRS)]
            for i in range(0, SUBLANES, 2):
                r0, r1 = slice(i, i + 1), slice(i + 1, i + 2)
                sums = [(_pair_dot(sp[p], a8_l[r0, pairs[p]], a8_r[r0, pairs[p]], left),
                         _pair_dot(sp[p], wa8_l[r0, pairs[p]], wa8_r[r0, pairs[p]], left)) for p in range(N_PAIRS)]
                sa0, sa1 = [s[0] for s in sums], [s[1] for s in sums]
                for p in range(N_PAIRS):
                    pr = pairs[p]
                    inner = slice((p % 2) * LANES, (p % 2 + 1) * LANES)
                    vt0 = vbuf[slot, p // 2, i * HEAD:(i + 1) * HEAD, inner]
                    vt1 = vbuf[slot, p // 2, (i + 1) * HEAD:(i + 2) * HEAD, inner]
                    sa_next = sa1[p] + sa0[p] * ba8[r0, pr] + vt0 * ka8[r0, pr]
                    s1 = sp[p] * w8[r0, pr] + sa0[p] * b8[r0, pr] + vt0 * k8[r0, pr]
                    st_ref[q * SUBLANES + i, p] = s1
                    _store_tile(qbuf, slot, p, i, s1 * r8[r0, pr])
                    s2 = s1 * w8[r1, pr] + sa_next * b8[r1, pr] + vt1 * k8[r1, pr]
                    st_ref[q * SUBLANES + i + 1, p] = s2
                    _store_tile(qbuf, slot, p, i + 1, s2 * r8[r1, pr])
                    sp[p] = s2
            for p in range(N_PAIRS):
                state[p] = sp[p]

        def y_rows(q, slot):
            for g2 in range(2):
                y_ref[rows_of(q), quads[g2]] = _diag_rows(qbuf[slot, g2], diag2, bd2, sub_row2)

        v_tiles(0, 0)

        def two_groups(j, carry):
            q0 = 2 * j
            v_tiles(q0 + 1, 1)
            chain(q0, 0)
            y_rows(jnp.maximum(q0 - 1, 0), 1)
            v_tiles(jnp.minimum(q0 + 2, groups - 1), 0)
            chain(q0 + 1, 1)
            y_rows(q0, 0)
            return carry

        lax.fori_loop(0, groups // 2, two_groups, 0)
        y_rows(groups - 1, 1)
        if moves:
            moves.wait(also=(g == nb - 1))

    rows = pl.BlockSpec((tb, D_HALF), lambda g: (g, 0))
    ex_in = exchange.operands if exchange else []
    ex_out = exchange.out_shapes if exchange else []
    res = pl.pallas_call(
        body, name="wkv_fwd", grid=(nb,),
        in_specs=[rows] * 6 + [ANY] * len(ex_in),
        out_specs=[rows, pl.BlockSpec((tb, N_PAIRS, HEAD, LANES), lambda g: (g, 0, 0, 0))] + [ANY] * len(ex_out),
        out_shape=[jax.ShapeDtypeStruct((s, D_HALF), F32),
                   jax.ShapeDtypeStruct((s, N_PAIRS, HEAD, LANES), F32)] + ex_out,
        scratch_shapes=[pltpu.VMEM((N_PAIRS, HEAD, LANES), F32),
                        pltpu.VMEM((2, 2, SUBLANES * HEAD, 2 * LANES), F32),
                        pltpu.VMEM((2, 2, SUBLANES * HEAD, 2 * LANES), BF16)]
                       + (exchange.scratch() if exchange else []),
        compiler_params=_params("arbitrary"),
    )(r, w, k, a, b, v, *ex_in)
    return res[0], res[1], list(res[2:])


def _wkv_bwd(r, w, k, a, b, v, dy, st, exchange=None):
    s = r.shape[0]
    tb = SCAN_TB
    nb = s // tb

    def body(*refs):
        ((r_ref, w_ref, k_ref, a_ref, b_ref, v_ref, dy_ref, st_ref, before_ref),
         (dr_ref, dw_ref, dk_ref, dv_ref, da_ref, db_ref), (dstate, vbuf, qbuf, sbuf),
         moves) = _split_refs(refs, 9, 6, exchange)
        g = pl.program_id(0)
        first_block = g == nb - 1
        if moves:
            moves.start(also=(g == 0))

        @pl.when(g == 0)
        def _():
            dstate[...] = jnp.zeros_like(dstate)
            qbuf[...] = jnp.zeros_like(qbuf)

        left = _left_half()
        diag2, bd2 = _quad_consts()
        sub_row = lax.broadcasted_iota(jnp.int32, (SUBLANES, LANES), 0)
        sub_row2 = lax.broadcasted_iota(jnp.int32, (SUBLANES, 2 * LANES), 0)
        groups = tb // SUBLANES
        quads = [slice(g2 * 2 * LANES, (g2 + 1) * 2 * LANES) for g2 in range(2)]
        row_refs = (dr_ref, dw_ref, dk_ref, da_ref, db_ref)

        def rows_of(q):
            return pl.ds(pl.multiple_of(q * SUBLANES, SUBLANES), SUBLANES)

        def state_before(q, i, p):
            if i > 0:
                return st_ref[q * SUBLANES + i - 1, p]
            return jnp.where(q == 0, jnp.where(first_block, 0.0, before_ref[0, p]),
                             st_ref[jnp.maximum(q * SUBLANES - 1, 0), p])

        def column_tiles(q, slot):
            rows8 = rows_of(q)
            for kind, ref in enumerate((v_ref, dy_ref)):
                x8 = ref[rows8, :]
                for g2 in range(2):
                    vbuf[slot, kind, g2] = _rows_to_columns(x8[:, quads[g2]], diag2, bd2)
            a8 = a_ref[rows8, :]
            for i in range(SUBLANES):
                for p in range(N_PAIRS):
                    _store_tile(sbuf, 0, p, i, state_before(q, i, p) * a8[i:i + 1, p * LANES:(p + 1) * LANES])
            for g2 in range(2):
                vbuf[slot, 2, g2] = jnp.dot(sbuf[0, g2], bd2, preferred_element_type=F32)

        def chain(q, slot):
            rows8 = rows_of(q)
            a8, w8, b8, k8, r8 = (x[rows8, :] for x in (a_ref, w_ref, b_ref, k_ref, r_ref))
            b8_l, b8_r = _halves(b8)
            dsp = [dstate[p] for p in range(N_PAIRS)]
            outs = [[jnp.zeros((SUBLANES, LANES), F32) for _ in row_refs] for _ in range(N_PAIRS)]
            after = [st_ref[q * SUBLANES + SUBLANES - 1, p] for p in range(N_PAIRS)]
            for i in reversed(range(SUBLANES)):
                row = slice(i, i + 1)
                pl_ = [slice(p * LANES, (p + 1) * LANES) for p in range(N_PAIRS)]
                tile = [(p // 2, slice(i * HEAD, (i + 1) * HEAD), slice((p % 2) * LANES, (p % 2 + 1) * LANES))
                        for p in range(N_PAIRS)]
                sp = [state_before(q, i, p) for p in range(N_PAIRS)]
                dyt = [vbuf[(slot, 1) + tile[p]] for p in range(N_PAIRS)]
                ds = [dsp[p] + dyt[p] * r8[row, pl_[p]] for p in range(N_PAIRS)]
                dsa = [_pair_dot(ds[p], b8_l[row, pl_[p]], b8_r[row, pl_[p]], left) for p in range(N_PAIRS)]
                sa = [vbuf[(slot, 2) + tile[p]] for p in range(N_PAIRS)]
                for p in range(N_PAIRS):
                    ar, wr, br, kr = (x[row, pl_[p]] for x in (a8, w8, b8, k8))
                    vt = vbuf[(slot, 0) + tile[p]]
                    dsp[p] = ds[p] * wr + dsa[p] * ar
                    new = (_colsum(after[p] * dyt[p]), _colsum(ds[p] * sp[p]), _colsum(ds[p] * vt),
                           _colsum(sp[p] * dsa[p]), _colsum(ds[p] * sa[p]))
                    outs[p] = [jnp.where(sub_row == i, n, o) for n, o in zip(new, outs[p])]
                    _store_tile(qbuf, slot, p, i, ds[p] * kr)
                after = sp
            for p in range(N_PAIRS):
                dstate[p] = dsp[p]
                for ref, o in zip(row_refs, outs[p]):
                    ref[rows8, p * LANES:(p + 1) * LANES] = o

        def dv_rows(q, slot):
            for g2 in range(2):
                dv_ref[rows_of(q), quads[g2]] = _diag_rows(qbuf[slot, g2], diag2, bd2, sub_row2)

        column_tiles(groups - 1, 0)

        def two_groups(j, carry):
            q0 = groups - 1 - 2 * j
            column_tiles(q0 - 1, 1)
            chain(q0, 0)
            dv_rows(jnp.minimum(q0 + 1, groups - 1), 1)
            column_tiles(jnp.maximum(q0 - 2, 0), 0)
            chain(q0 - 1, 1)
            dv_rows(q0, 0)
            return carry

        lax.fori_loop(0, groups // 2, two_groups, 0)
        dv_rows(0, 1)
        if moves:
            moves.wait(also=(g == nb - 1))

    rows = pl.BlockSpec((tb, D_HALF), lambda g: (nb - 1 - g, 0))
    ex_in = exchange.operands if exchange else []
    ex_out = exchange.out_shapes if exchange else []
    res = pl.pallas_call(
        body, name="wkv_bwd", grid=(nb,),
        in_specs=[rows] * 7 + [pl.BlockSpec((tb, N_PAIRS, HEAD, LANES), lambda g: (nb - 1 - g, 0, 0, 0)),
                               pl.BlockSpec((1, N_PAIRS, HEAD, LANES),
                                            lambda g: (jnp.maximum((nb - 1 - g) * tb - 1, 0), 0, 0, 0))]
                 + [ANY] * len(ex_in),
        out_specs=[rows] * 6 + [ANY] * len(ex_out),
        out_shape=[jax.ShapeDtypeStruct((s, D_HALF), F32)] * 6 + ex_out,
        scratch_shapes=[pltpu.VMEM((N_PAIRS, HEAD, LANES), F32),
                        pltpu.VMEM((2, 3, 2, SUBLANES * HEAD, 2 * LANES), F32),
                        pltpu.VMEM((2, 2, SUBLANES * HEAD, 2 * LANES), BF16),
                        pltpu.VMEM((1, 2, SUBLANES * HEAD, 2 * LANES), BF16)]
                       + (exchange.scratch() if exchange else []),
        compiler_params=_params("arbitrary"),
    )(r, w, k, a, b, v, dy, st, st, *ex_in)
    return list(res[:6]), list(res[6:])


def _rwkv_post_math(y, r, k, v, gate, lw, lb, rk, bd):
    mean = _head_sum(y, bd) * (1.0 / HEAD)
    yc = y - mean
    var = _head_sum(yc * yc, bd) * (1.0 / HEAD)
    rstd = lax.rsqrt(var + LNX_EPS)
    yn = yc * rstd
    rkk = _head_sum(r * k * rk, bd)
    sg = _sigmoid(gate)
    pre = yn * lw + lb + rkk * v
    return yn, rstd, rkk, sg, pre


def _rwkv_prep_bwd(u_a, h_t, grads, mu, wl, w0, a0, kkw, kaw, tm=256):
    s = u_a.shape[0]
    nb = s // tm
    d = h_t.shape[0]

    def body(ua_ref, prev_ref, ht_ref, drs_ref, dws_ref, dks_ref, dvs_ref, das_ref, dbs_ref, drb_ref, dkb_ref, dvb_ref,
             dgt_ref, mu_ref, wl_ref, w0_ref, a0_ref, kkw_ref, kaw_ref,
             du_ref, dwa_ref, dmu_ref, dwl_ref, dw0_ref, da0_ref, dkkw_ref, dkaw_ref, carry):
        i = pl.program_id(0)

        @pl.when(i == 0)
        def _():
            carry[...] = jnp.zeros_like(carry)
            for ref in (dwa_ref, dmu_ref, dwl_ref, dw0_ref, da0_ref, dkkw_ref, dkaw_ref):
                ref[...] = jnp.zeros_like(ref)

        bd = _head_ones()
        mu_v, wl_v, kkw_v, kaw_v = mu_ref[...], wl_ref[...], kkw_ref[...], kaw_ref[...]
        f = _rwkv_elementwise(ua_ref[...], prev_ref[7:8, :], i == nb - 1, mu_v, wl_v, w0_ref[...],
                              a0_ref[...], kkw_v, kaw_v, bd)
        a, kk, k0 = f["a"], f["kk"], f["k0"]
        dk = dks_ref[...] + dkb_ref[...]
        dbs = dbs_ref[...]
        dkk = dbs * a - das_ref[...]
        da = dbs * kk + dk * k0 * kaw_v
        dk0 = dk * (1.0 + (a - 1.0) * kaw_v)
        dkaw_ref[...] += _colsum(dk * k0 * (a - 1.0))
        inv = 1.0 / f["nrm"]
        proj = _head_sum(dkk * kk, bd)
        dkk0 = jnp.where(f["ss"] > 1e-24, (dkk - kk * proj) * inv, dkk * inv)
        dk0 = dk0 + dkk0 * kkw_v
        dkkw_ref[...] += _colsum(dkk0 * k0)
        dza = da * a * (1.0 - a)
        da0_ref[...] += _colsum(dza)
        dz = -dws_ref[...] * f["dec"] * f["e"] * (1.0 - f["sz"])
        dw0_ref[...] += _colsum(dz)
        dll = jnp.concatenate([dz, dza], axis=1).astype(BF16)
        dwl_ref[...] += _dot_tn(f["lin"].astype(BF16), dll)
        dlin = _dot_nt(dll, wl_v)
        lane = lax.broadcasted_iota(jnp.int32, (1, LANES), 1)
        th = f["th"]
        dlo = jnp.where(lane < LORA, dlin * (1.0 - th * th), dlin)
        dus = jnp.concatenate([drs_ref[...] + drb_ref[...], dk0, dvs_ref[...] + dvb_ref[...], dlo, dgt_ref[...]],
                              axis=1)
        dmu_ref[...] += _colsum(dus * f["delta"])
        g1 = dus * mu_v
        rows = lax.broadcasted_iota(jnp.int32, (tm, 1), 0)
        up = jnp.where(rows == tm - 1, carry[...], pltpu.roll(g1, tm - 1, 0))
        dua = dus - g1 + up
        du_ref[...] = dua
        dwa_ref[...] += jnp.dot(ht_ref[...], dua.astype(BF16), preferred_element_type=F32)
        carry[...] = g1[0:1, :]

    rev = lambda w: pl.BlockSpec((tm, w), lambda i: (nb - 1 - i, 0))
    vec = lambda w: pl.BlockSpec((1, w), lambda i: (0, 0))
    wl_spec = pl.BlockSpec((LANES, 2 * D_HALF), lambda i: (0, 0))
    return pl.pallas_call(
        body, name="rwkv_prep_bwd", grid=(nb,),
        in_specs=[rev(SEC), pl.BlockSpec((8, SEC), lambda i: (jnp.maximum((nb - 1 - i) * (tm // 8) - 1, 0), 0)),
                  pl.BlockSpec((d, tm), lambda i: (0, nb - 1 - i))]
                 + [rev(D_HALF)] * 10 + [vec(SEC), wl_spec] + [vec(D_HALF)] * 4,
        out_specs=[rev(SEC), pl.BlockSpec((d, SEC), lambda i: (0, 0)), vec(SEC), wl_spec] + [vec(D_HALF)] * 4,
        out_shape=[jax.ShapeDtypeStruct((s, SEC), F32), jax.ShapeDtypeStruct((d, SEC), F32),
                   jax.ShapeDtypeStruct((1, SEC), F32),
                   jax.ShapeDtypeStruct((LANES, 2 * D_HALF), F32)] + [jax.ShapeDtypeStruct((1, D_HALF), F32)] * 4,
        scratch_shapes=[pltpu.VMEM((1, SEC), F32)],
        compiler_params=_params("arbitrary"),
    )(u_a, u_a, h_t, *grads, mu, wl, w0, a0, kkw, kaw)


def _tri(tm, lower):
    r = lax.broadcasted_iota(jnp.int32, (tm, tm), 0)
    c = lax.broadcasted_iota(jnp.int32, (tm, tm), 1)
    return ((r >= c) if lower else (r <= c)).astype(BF16)


def _head_rms(x, g, bd):
    rinv = lax.rsqrt(_head_sum(x * x, bd) * (1.0 / HEAD) + RMS_EPS)
    xh = x * rinv
    return xh, rinv, xh * g


def _fox_prep(u_b, fb, qg, kg, tm=256):
    s = u_b.shape[0]

    def body(ub_ref, fb_ref, qg_ref, kg_ref, q_ref, k_ref, v_ref, cc_ref, cr_ref, carry):
        i = pl.program_id(0)

        @pl.when(i == 0)
        def _():
            carry[...] = jnp.zeros_like(carry)

        bd = _head_ones()
        _, _, qn = _head_rms(ub_ref[:, 0:512], qg_ref[...], bd)
        _, _, kn = _head_rms(ub_ref[:, 512:1024], kg_ref[...], bd)
        q_ref[...] = (qn * ATT_SCALE).astype(BF16)
        k_ref[...] = kn.astype(BF16)
        v_ref[...] = ub_ref[:, 1024:1536].astype(BF16)
        lane = lax.broadcasted_iota(jnp.int32, (1, LANES), 1)
        logf = jnp.where(lane < N_HEADS, _log_sigmoid(ub_ref[:, 2048:2176] + fb_ref[...]), 0.0)
        cum = _exact_dot(logf, _tri(tm, True), ones_first=True) + carry[...]
        for h in range(N_HEADS):
            cc_ref[h] = jnp.broadcast_to(cum[:, h:h + 1], (tm, LANES))
        cr_ref[...] = jnp.transpose(cum)[0:N_HEADS, :]
        carry[...] = cum[tm - 1:tm, :]

    blk = pl.BlockSpec((tm, D_HALF), lambda i: (i, 0))
    return pl.pallas_call(
        body, name="fox_prep", grid=(s // tm,),
        in_specs=[pl.BlockSpec((tm, SEC), lambda i: (i, 0)), pl.BlockSpec((1, LANES), lambda i: (0, 0)),
                  pl.BlockSpec((1, D_HALF), lambda i: (0, 0)), pl.BlockSpec((1, D_HALF), lambda i: (0, 0))],
        out_specs=[blk, blk, blk, pl.BlockSpec((N_HEADS, tm, LANES), lambda i: (0, i, 0)),
                   pl.BlockSpec((N_HEADS, tm), lambda i: (0, i))],
        out_shape=[jax.ShapeDtypeStruct((s, D_HALF), BF16)] * 3
                  + [jax.ShapeDtypeStruct((N_HEADS, s, LANES), F32), jax.ShapeDtypeStruct((N_HEADS, s), F32)],
        scratch_shapes=[pltpu.VMEM((1, LANES), F32)],
        compiler_params=_params("arbitrary"),
    )(u_b, fb, qg, kg)


ATT_T = 256


def _attn_fwd(q, k, v, cc, cr):
    s = q.shape[0]
    t = ATT_T
    nblk = s // t

    def body(q_ref, k_ref, v_ref, cc_ref, cr_ref, o_ref, lse_ref, m_sc, l_sc, acc_sc):
        i = pl.program_id(0)
        j = pl.program_id(1)

        @pl.when(j == 0)
        def _():
            m_sc[...] = jnp.full_like(m_sc, NEG)
            l_sc[...] = jnp.zeros_like(l_sc)
            acc_sc[...] = jnp.zeros_like(acc_sc)

        def tile(on_diagonal):
            causal = _causal_tile(t) if on_diagonal else None
            left = lax.broadcasted_iota(jnp.int32, (1, LANES), 1) < HEAD
            for p in range(N_PAIRS):
                lanes = slice(p * LANES, (p + 1) * LANES)
                q2, k2, v2 = q_ref[:, lanes], k_ref[:, lanes], v_ref[:, lanes]
                acc2 = acc_sc[:, lanes]
                for e in range(2):
                    h = 2 * p + e
                    msk = left if e == 0 else jnp.logical_not(left)
                    sc = _dot_nt(jnp.where(msk, q2, jnp.zeros_like(q2)), k2)
                    sc = sc + (_wide(cc_ref[h]) - cr_ref[h:h + 1, :])
                    if on_diagonal:
                        sc = jnp.where(causal, sc, NEG)
                    m_prev = m_sc[h]
                    m_new = jnp.maximum(m_prev, jnp.max(sc, axis=1, keepdims=True))
                    alpha = jnp.exp(m_prev - m_new)
                    pm = jnp.exp(sc - _wide(m_new))
                    l_sc[h] = alpha * l_sc[h] + jnp.sum(pm, axis=1, keepdims=True)
                    m_sc[h] = m_new
                    pv = jnp.dot(pm.astype(BF16), v2, preferred_element_type=F32)
                    acc2 = jnp.where(msk, alpha * acc2 + pv, acc2)
                acc_sc[:, lanes] = acc2

        pl.when(j < i)(functools.partial(tile, False))
        pl.when(j == i)(functools.partial(tile, True))

        @pl.when(j == i)
        def _():
            left = lax.broadcasted_iota(jnp.int32, (1, LANES), 1) < HEAD
            for p in range(N_PAIRS):
                lanes = slice(p * LANES, (p + 1) * LANES)
                inv = jnp.where(left, 1.0 / l_sc[2 * p], 1.0 / l_sc[2 * p + 1])
                o_ref[:, lanes] = acc_sc[:, lanes] * inv
            for h in range(N_HEADS):
                lse_ref[h] = m_sc[h] + jnp.log(l_sc[h])

    qblk = pl.BlockSpec((t, D_HALF), lambda i, j: (i, 0))
    kblk = pl.BlockSpec((t, D_HALF), lambda i, j: (jnp.minimum(i, j), 0))
    return pl.pallas_call(
        body, name="fox_attn_fwd", grid=(nblk, nblk),
        in_specs=[qblk, kblk, kblk, pl.BlockSpec((N_HEADS, t, LANES), lambda i, j: (0, i, 0)),
                  pl.BlockSpec((N_HEADS, t), lambda i, j: (0, jnp.minimum(i, j)))],
        out_specs=[qblk, pl.BlockSpec((N_HEADS, t, LANES), lambda i, j: (0, i, 0))],
        out_shape=[jax.ShapeDtypeStruct((s, D_HALF), F32), jax.ShapeDtypeStruct((N_HEADS, s, LANES), F32)],
        scratch_shapes=[pltpu.VMEM((N_HEADS, t, LANES), F32), pltpu.VMEM((N_HEADS, t, LANES), F32),
                        pltpu.VMEM((t, D_HALF), F32)],
        compiler_params=_params("parallel", "arbitrary"),
    )(q, k, v, cc, cr)


def _causal_tile(t):
    return lax.broadcasted_iota(jnp.int32, (t, t), 0) >= lax.broadcasted_iota(jnp.int32, (t, t), 1)


def _wide(x):
    return jnp.concatenate([x, x], axis=1)


def _attn_probs(q2, k2, v2, do2, msk, causal, bias, lse_rows):
    zero = jnp.zeros_like(q2)
    qh = jnp.where(msk, q2, zero)
    doh = jnp.where(msk, do2, zero)
    sc = _dot_nt(qh, k2) + bias
    if causal is not None:
        sc = jnp.where(causal, sc, NEG)
    pm = jnp.exp(sc - _wide(lse_rows))
    dp = _dot_nt(doh, v2)
    return qh, doh, pm, dp


def _attn_bwd_rowdot(q, k, v, do, lse, cc, cr):
    s = q.shape[0]
    t = ATT_T
    nblk = s // t

    def body(q_ref, k_ref, v_ref, do_ref, lse_ref, cc_ref, cr_ref, dd_ref, acc):
        i = pl.program_id(0)
        j = pl.program_id(1)

        @pl.when(j == 0)
        def _():
            acc[...] = jnp.zeros_like(acc)

        def tile(on_diagonal):
            causal = _causal_tile(t) if on_diagonal else None
            left = lax.broadcasted_iota(jnp.int32, (1, LANES), 1) < HEAD
            for p in range(N_PAIRS):
                lanes = slice(p * LANES, (p + 1) * LANES)
                q2, k2, v2, do2 = q_ref[:, lanes], k_ref[:, lanes], v_ref[:, lanes], do_ref[:, lanes]
                for e in range(2):
                    h = 2 * p + e
                    msk = left if e == 0 else jnp.logical_not(left)
                    bias = _wide(cc_ref[h]) - cr_ref[h:h + 1, :]
                    _, _, pm, dp = _attn_probs(q2, k2, v2, do2, msk, causal, bias, lse_ref[h])
                    acc[h] += jnp.sum(pm * dp, axis=1, keepdims=True)

        pl.when(j < i)(functools.partial(tile, False))
        pl.when(j == i)(functools.partial(tile, True))

        @pl.when(j == i)
        def _():
            dd_ref[...] = acc[...]

    qblk = pl.BlockSpec((t, D_HALF), lambda i, j: (i, 0))
    qcol = pl.BlockSpec((N_HEADS, t, LANES), lambda i, j: (0, i, 0))
    kblk = pl.BlockSpec((t, D_HALF), lambda i, j: (jnp.minimum(i, j), 0))
    return pl.pallas_call(
        body, name="fox_attn_rowdot", grid=(nblk, nblk),
        in_specs=[qblk, kblk, kblk, qblk, qcol, qcol, pl.BlockSpec((N_HEADS, t), lambda i, j: (0, jnp.minimum(i, j)))],
        out_specs=qcol, out_shape=jax.ShapeDtypeStruct((N_HEADS, s, LANES), F32),
        scratch_shapes=[pltpu.VMEM((N_HEADS, t, LANES), F32)],
        compiler_params=_params("parallel", "arbitrary"),
    )(q, k, v, do, lse, cc, cr)


def _attn_bwd(q, k, v, do, lse, dd, cc, cr):
    s = q.shape[0]
    t = ATT_T
    nblk = s // t

    def body(q_ref, k_ref, v_ref, do_ref, lse_ref, dd_ref, cc_ref, cr_ref,
             dq_ref, dk_ref, dv_ref, dcr_ref, dk_sc, dv_sc, dcr_sc):
        j = pl.program_id(0)
        i = pl.program_id(1)

        @pl.when(jnp.logical_and(j == 0, i == 0))
        def _():
            dq_ref[...] = jnp.zeros_like(dq_ref)

        @pl.when(i == 0)
        def _():
            dk_sc[...] = jnp.zeros_like(dk_sc)
            dv_sc[...] = jnp.zeros_like(dv_sc)
            dcr_sc[...] = jnp.zeros_like(dcr_sc)

        def tile(on_diagonal):
            causal = _causal_tile(t) if on_diagonal else None
            left = lax.broadcasted_iota(jnp.int32, (1, LANES), 1) < HEAD
            qrows = pl.ds(pl.multiple_of(i * t, t), t)
            for p in range(N_PAIRS):
                lanes = slice(p * LANES, (p + 1) * LANES)
                q2, k2, v2, do2 = q_ref[:, lanes], k_ref[:, lanes], v_ref[:, lanes], do_ref[:, lanes]
                zero = jnp.zeros_like(q2)
                dq2 = jnp.zeros((t, LANES), F32)
                dk2 = jnp.zeros((t, LANES), F32)
                dv2 = jnp.zeros((t, LANES), F32)
                for e in range(2):
                    h = 2 * p + e
                    msk = left if e == 0 else jnp.logical_not(left)
                    bias = _wide(cc_ref[h]) - cr_ref[h:h + 1, :]
                    qh, doh, pm, dp = _attn_probs(q2, k2, v2, do2, msk, causal, bias, lse_ref[h])
                    dsc = pm * (dp - _wide(dd_ref[h]))
                    dsb = dsc.astype(BF16)
                    dv2 += _dot_tn(pm.astype(BF16), doh)
                    dk2 += _dot_tn(dsb, qh)
                    dq2 += jnp.dot(dsb, jnp.where(msk, k2, zero), preferred_element_type=F32)
                    dcr_sc[h:h + 1, :] += -_colsum(dsc)
                dq_ref[qrows, lanes] += dq2 * ATT_SCALE
                dk_sc[:, lanes] += dk2
                dv_sc[:, lanes] += dv2

        pl.when(i > j)(functools.partial(tile, False))
        pl.when(i == j)(functools.partial(tile, True))

        @pl.when(i == nblk - 1)
        def _():
            dk_ref[...] = dk_sc[...]
            dv_ref[...] = dv_sc[...]
            dcr_ref[...] = dcr_sc[...]

    qblk = pl.BlockSpec((t, D_HALF), lambda j, i: (jnp.maximum(i, j), 0))
    qcol = pl.BlockSpec((N_HEADS, t, LANES), lambda j, i: (0, jnp.maximum(i, j), 0))
    kblk = pl.BlockSpec((t, D_HALF), lambda j, i: (j, 0))
    return pl.pallas_call(
        body, name="fox_attn_bwd", grid=(nblk, nblk),
        in_specs=[qblk, kblk, kblk, qblk, qcol, qcol, qcol, pl.BlockSpec((N_HEADS, t), lambda j, i: (0, j))],
        out_specs=[pl.BlockSpec((s, D_HALF), lambda j, i: (0, 0)), kblk, kblk,
                   pl.BlockSpec((N_HEADS, t), lambda j, i: (0, j))],
        out_shape=[jax.ShapeDtypeStruct((s, D_HALF), F32)] * 3 + [jax.ShapeDtypeStruct((N_HEADS, s), F32)],
        scratch_shapes=[pltpu.VMEM((t, D_HALF), F32), pltpu.VMEM((t, D_HALF), F32), pltpu.VMEM((N_HEADS, t), F32)],
        compiler_params=_params("arbitrary", "arbitrary"),
    )(q, k, v, do, lse, dd, cc, cr)


def _fox_prep_bwd(u_b, h_t, dq, dk, dv, dgate, dcum, fb, qg, kg, tm=256):
    s = u_b.shape[0]
    nb = s // tm
    d = h_t.shape[0]

    def body(ub_ref, ht_ref, dq_ref, dk_ref, dv_ref, dg_ref, dc_ref, fb_ref, qg_ref, kg_ref,
             du_ref, dwb_ref, dqg_ref, dkg_ref, dfb_ref, carry):
        i = pl.program_id(0)

        @pl.when(i == 0)
        def _():
            carry[...] = jnp.zeros_like(carry)
            dwb_ref[...] = jnp.zeros_like(dwb_ref)
            dqg_ref[...] = jnp.zeros_like(dqg_ref)
            dkg_ref[...] = jnp.zeros_like(dkg_ref)
            dfb_ref[...] = jnp.zeros_like(dfb_ref)

        bd = _head_ones()
        for lo, g_ref, d_ref, dgain_ref in ((0, qg_ref, dq_ref, dqg_ref), (512, kg_ref, dk_ref, dkg_ref)):
            gain = g_ref[...]
            xh, rinv, _ = _head_rms(ub_ref[:, lo:lo + 512], gain, bd)
            dn = d_ref[...]
            dgain_ref[...] += _colsum(dn * xh)
            dxh = dn * gain
            du_ref[:, lo:lo + 512] = rinv * (dxh - xh * (_head_sum(dxh * xh, bd) * (1.0 / HEAD)))
        du_ref[:, 1024:1536] = dv_ref[...]
        du_ref[:, 1536:2048] = dg_ref[...]
        lane = lax.broadcasted_iota(jnp.int32, (1, LANES), 1)
        dc = dc_ref[...]
        dlogf = _exact_dot(dc, _tri(tm, False), ones_first=True) + carry[...]
        carry[...] += _colsum(dc)
        fl = ub_ref[:, 2048:2176] + fb_ref[...]
        dfl = jnp.where(lane < N_HEADS, dlogf * (1.0 - _sigmoid(fl)), 0.0)
        du_ref[:, 2048:2176] = dfl
        dfb_ref[...] += _colsum(dfl)
        dwb_ref[...] += jnp.dot(ht_ref[...], du_ref[...].astype(BF16), preferred_element_type=F32)

    rev = lambda w: pl.BlockSpec((tm, w), lambda i: (nb - 1 - i, 0))
    vec = lambda w: pl.BlockSpec((1, w), lambda i: (0, 0))
    return pl.pallas_call(
        body, name="fox_prep_bwd", grid=(nb,),
        in_specs=[rev(SEC), pl.BlockSpec((d, tm), lambda i: (0, nb - 1 - i))] + [rev(D_HALF)] * 4
                 + [rev(LANES), vec(LANES), vec(D_HALF), vec(D_HALF)],
        out_specs=[rev(SEC), pl.BlockSpec((d, SEC), lambda i: (0, 0)), vec(D_HALF), vec(D_HALF), vec(LANES)],
        out_shape=[jax.ShapeDtypeStruct((s, SEC), F32), jax.ShapeDtypeStruct((d, SEC), F32),
                   jax.ShapeDtypeStruct((1, D_HALF), F32), jax.ShapeDtypeStruct((1, D_HALF), F32),
                   jax.ShapeDtypeStruct((1, LANES), F32)],
        scratch_shapes=[pltpu.VMEM((1, LANES), F32)],
        compiler_params=_params("arbitrary"),
    )(u_b, h_t, dq, dk, dv, dgate, dcum, fb, qg, kg)


def _merge(y, r, k, v, gate_a, o, u_b, h, x, tgt, w_g, wa, wb, wo, fg, lw, lb, rk, tm=256):
    s, d = x.shape

    def body(y_ref, r_ref, k_ref, v_ref, ga_ref, o_ref, gb_ref, h_ref, x_ref, t_ref, wg_ref, wa_ref, wb_ref, wo_ref,
             fg_ref, lw_ref, lb_ref, rk_ref,
             dx2_ref, dy_ref, drb_ref, dkb_ref, dvb_ref, dga_ref, do_ref, dgb_ref, dug_ref,
             dwa_ref, dwb_ref, dwo_ref, dfg_ref, loss_ref, dlw_ref, dlb_ref, drk_ref):
        i = pl.program_id(0)

        @pl.when(i == 0)
        def _():
            for ref in (dwa_ref, dwb_ref, dwo_ref, dfg_ref, loss_ref, dlw_ref, dlb_ref, drk_ref):
                ref[...] = jnp.zeros_like(ref)

        bd = _head_ones()
        wa_v, wb_v, wo_v, fg_v = wa_ref[...], wb_ref[...], wo_ref[...], fg_ref[...]
        rv, kv, vv, ga, lw_v, rk_v = r_ref[...], k_ref[...], v_ref[...], ga_ref[...], lw_ref[...], rk_ref[...]
        yn, rstd, rkk, sga, pre = _rwkv_post_math(y_ref[...], rv, kv, vv, ga, lw_v, lb_ref[...], rk_v, bd)
        silu_a = ga * sga
        gb, ov = gb_ref[...], o_ref[...]
        sgb = _sigmoid(gb)
        silu_b = gb * sgb
        ma = (pre * silu_a).astype(BF16)
        mb = (ov * silu_b).astype(BF16)
        ya = jnp.dot(ma, wa_v, preferred_element_type=F32)
        yb = jnp.dot(mb, wb_v, preferred_element_type=F32)
        ug = jnp.dot(h_ref[...], wg_ref[...], preferred_element_type=F32)
        sa = _sigmoid(ug[:, 0:d])
        sb = _sigmoid(ug[:, d:2 * d])
        merged = (sa * ya + sb * yb).astype(BF16)
        x2 = x_ref[...] + jnp.dot(merged, wo_v, preferred_element_type=F32)
        r2 = lax.rsqrt(jnp.mean(x2 * x2, axis=-1, keepdims=True) + RMS_EPS)
        x2h = x2 * r2
        err = x2h * fg_v - t_ref[...]
        loss_ref[...] += _colsum(err * err)
        dyo = err * (1.0 / d)
        dfg_ref[...] += _colsum(dyo * x2h)
        dx2h = dyo * fg_v
        dx2 = r2 * (dx2h - x2h * jnp.mean(dx2h * x2h, axis=-1, keepdims=True))
        dx2_ref[...] = dx2
        dx2b = dx2.astype(BF16)
        dmerged = _dot_nt(dx2b, wo_v)
        dwo_ref[...] += _dot_tn(merged, dx2b)
        dya = dmerged * sa
        dyb = dmerged * sb
        dug_ref[:, 0:d] = dya * ya * (1.0 - sa)
        dug_ref[:, d:2 * d] = dyb * yb * (1.0 - sb)
        dyab = dya.astype(BF16)
        dybb = dyb.astype(BF16)
        dwa_ref[...] += _dot_tn(ma, dyab)
        dwb_ref[...] += _dot_tn(mb, dybb)
        dmb = _dot_nt(dybb, wb_v)
        do_ref[...] = (dmb * silu_b).astype(BF16)
        dgb_ref[...] = dmb * ov * (sgb * (1.0 + gb * (1.0 - sgb)))
        dma = _dot_nt(dyab, wa_v)
        dga_ref[...] = dma * pre * (sga * (1.0 + ga * (1.0 - sga)))
        dpre = dma * silu_a
        dlw_ref[...] += _colsum(dpre * yn)
        dlb_ref[...] += _colsum(dpre)
        dyn = dpre * lw_v
        m1 = _head_sum(dyn, bd) * (1.0 / HEAD)
        m2 = _head_sum(dyn * yn, bd) * (1.0 / HEAD)
        dy_ref[...] = rstd * (dyn - m1 - yn * m2)
        dvb_ref[...] = dpre * rkk
        drkk = _head_sum(dpre * vv, bd)
        drb_ref[...] = drkk * kv * rk_v
        dkb_ref[...] = drkk * rv * rk_v
        drk_ref[...] += _colsum(drkk * rv * kv)

    row = lambda w: pl.BlockSpec((tm, w), lambda i: (i, 0))
    full = lambda a: pl.BlockSpec(a.shape, lambda i: (0, 0))
    once = lambda a: pl.BlockSpec(a.shape, lambda i: (0, 0), pipeline_mode=pl.Buffered(1))
    half = jax.ShapeDtypeStruct((s, D_HALF), F32)
    fshape = lambda a: jax.ShapeDtypeStruct(a.shape, F32)
    return pl.pallas_call(
        body, name="merge_fwd_bwd", grid=(s // tm,),
        in_specs=[row(D_HALF)] * 6 + [pl.BlockSpec((tm, D_HALF), lambda i: (i, 3)), row(d), row(d), row(d),
                                      once(w_g), once(wa), once(wb), once(wo), full(fg), full(lw), full(lb), full(rk)],
        out_specs=[row(d)] + [row(D_HALF)] * 7 + [row(GATE_COLS), full(wa), full(wb), full(wo), full(fg), full(fg),
                                                   full(lw), full(lb), full(rk)],
        out_shape=[jax.ShapeDtypeStruct((s, d), F32)] + [half] * 5 + [jax.ShapeDtypeStruct((s, D_HALF), BF16), half,
                                                                    jax.ShapeDtypeStruct((s, GATE_COLS), F32),
                                                                    fshape(wa), fshape(wb), fshape(wo), fshape(fg),
                                                                    fshape(fg), fshape(lw), fshape(lb), fshape(rk)],
        compiler_params=_params("arbitrary"),
    )(y, r, k, v, gate_a, o, u_b, h, x, tgt, w_g, wa, wb, wo, fg, lw, lb, rk)


def _lora_weight(w_up, a_up):
    z = jnp.zeros((LORA, D_HALF), w_up.dtype)
    return jnp.concatenate([jnp.concatenate([w_up, z], axis=1), jnp.concatenate([z, a_up], axis=1)], axis=0)


def _device_grads(x, tgt, p, w_a, w_up, a_up, late_weights, fwd_exchange=None, bwd_exchange=None, tail_exchange=None):
    wl = _lora_weight(w_up, a_up)
    rk = p["r_k"].reshape(1, D_HALF)
    fb = jnp.pad(p["f_bias"], ((0, 0), (0, LANES - N_HEADS)))
    qg = jnp.tile(p["q_norm_g"], (1, N_HEADS))
    kg = jnp.tile(p["k_norm_g"], (1, N_HEADS))
    fg = p["final_norm_g"].reshape(1, D_MODEL)
    mixer = (p["shift_mu"], wl, p["w0"], p["a0"], p["k_k"], p["k_a"])

    h = _rmsnorm_in(x, p["norm_g"])
    u_a = _matmul_nn(h, w_a, "inproj_rwkv")
    r, dec, k, v, av, bv, gate_a = _rwkv_prep(u_a, *mixer)
    y, st, arrived = _wkv_fwd(r, dec, k, av, bv, v, fwd_exchange)

    w_b, w_g, w_out_a, w_out_b, w_out = late_weights(arrived)
    u_b = _matmul_nn(h, w_b, "inproj_fox")
    q, kn, vb, cc, cr = _fox_prep(u_b, fb, qg, kg)
    o, lse = _attn_fwd(q, kn, vb, cc, cr)

    (dx2, dy, dr_b, dk_b, dv_b, dgate_a, do, dgate_b, du_g, dwa, dwb, dwo, dfg, loss_vec, dlw, dlb, drk) = _merge(
        y, r, k, v, gate_a, o, u_b, h, x, tgt, w_g, w_out_a, w_out_b, w_out, fg, p["lnx_w"], p["lnx_b"], rk)

    dd = _attn_bwd_rowdot(q, kn, vb, do, lse, cc, cr)
    dq, dk_att, dv_att, dcr = _attn_bwd(q, kn, vb, do, lse, dd, cc, cr)
    dcum = jnp.pad(dcr.T, ((0, 0), (0, LANES - N_HEADS)))
    h_t = h.T
    du_b, dw_b, dqg, dkg, dfb = _fox_prep_bwd(u_b, h_t, dq, dk_att, dv_att, dgate_b, dcum, fb, qg, kg)
    dw_g = _matmul_tn_acc(h_t, du_g, "dw_gate")

    scan_grads, sent = _wkv_bwd(r, dec, k, av, bv, v, dy, st,
                                bwd_exchange(dw_b, dw_g, dwa, dwb, dwo) if bwd_exchange else None)
    du_a, dw_a, dmu, dwl, dw0, da0, dkkw, dkaw = _rwkv_prep_bwd(
        u_a, h_t, (*scan_grads, dr_b, dk_b, dv_b, dgate_a), *mixer)
    dw_up, da_up = dwl[:LORA, :D_HALF], dwl[LORA:, D_HALF:]
    sent_last = _run_on_sequencer(tail_exchange(dw_a, dw_up, da_up), "scatter_tail", 1) if tail_exchange else []
    grad_x, dnorm_g, _ = _inproj_bwd(du_a, du_b, du_g, w_a, w_b, w_g, x, dx2, p["norm_g"])

    grads = dict(
        norm_g=dnorm_g, w_in=(dw_a, dw_b, dw_g), shift_mu=dmu,
        w_lora_up=dw_up, w0=dw0, a_lora_up=da_up, a0=da0, k_k=dkkw, k_a=dkaw,
        r_k=drk.reshape(1, N_HEADS, HEAD), lnx_w=dlw, lnx_b=dlb, f_bias=dfb[:, :N_HEADS],
        q_norm_g=dqg.reshape(N_HEADS, HEAD).sum(axis=0, keepdims=True),
        k_norm_g=dkg.reshape(N_HEADS, HEAD).sum(axis=0, keepdims=True),
        w_out_a=dwa, w_out_b=dwb, w_out=dwo, final_norm_g=dfg.reshape(D_MODEL))
    return loss_vec, grad_x, grads, sent, sent_last


CHIP_FLIPS = ((1, 0), (0, 1), (1, 1))
ANY = pl.BlockSpec(memory_space=pl.ANY)


def _position():
    return lax.axis_index("x"), lax.axis_index("y"), lax.axis_index("c")


def _flip(v, f):
    return 1 - v if f else v


def _both(a, b):
    if a is None:
        return b
    return a if b is None else jnp.logical_and(a, b)


def _when(cond, fn):
    if cond is None:
        fn()
    else:
        pl.when(cond)(fn)


class _Moves:
    def __init__(self, send_sems, recv_sems, local_sems):
        self.send_sems, self.recv_sems, self.local_sems = send_sems, recv_sems, local_sems
        self.remote, self.local = [], []

    def send(self, src, dst, peer, landing, send_if=None, recv_if=None):
        k = len(self.remote)
        sems = dict(send_sem=self.send_sems.at[k], recv_sem=self.recv_sems.at[k], device_id=peer, device_id_type=MESH)
        out = pltpu.make_async_remote_copy(src_ref=src, dst_ref=dst, **sems)
        arrival = pltpu.make_async_remote_copy(src_ref=src, dst_ref=landing, **sems)
        self.remote.append((out, arrival, send_if, recv_if))

    def copy(self, src, dst, cond=None):
        cp = pltpu.make_async_copy(src, dst, self.local_sems.at[len(self.local)])
        self.local.append((cp, cond))

    def start(self, also=None):
        for cp, cond in self.local:
            _when(_both(also, cond), cp.start)
        for out, _, send_if, _ in self.remote:
            _when(_both(also, send_if), out.start)

    def wait_arrivals(self, also=None):
        for _, arrival, _, recv_if in self.remote:
            _when(_both(also, recv_if), arrival.wait_recv)

    def wait_sent(self, also=None):
        for out, _, send_if, _ in self.remote:
            _when(_both(also, send_if), out.wait_send)
        for cp, cond in self.local:
            _when(_both(also, cond), cp.wait)

    def wait(self, also=None):
        self.wait_arrivals(also)
        self.wait_sent(also)


class _Exchange:
    def __init__(self, operands, out_shapes, n_remote, n_local, build, n_relay=0, relay=None):
        self.operands, self.out_shapes = list(operands), list(out_shapes)
        self.n_remote, self.n_local, self.build = n_remote, n_local, build
        self.n_relay, self.relay = n_relay, relay

    def scratch(self):
        return [pltpu.SemaphoreType.DMA((self.n_remote,)), pltpu.SemaphoreType.DMA((self.n_remote,)),
                pltpu.SemaphoreType.DMA((max(self.n_local, 1),))]

    def moves(self, in_refs, out_refs, sems):
        mv = _Moves(*sems)
        self.build(mv, in_refs, out_refs)
        return mv

    def run_alone(self, name):
        n_in, n_out = len(self.operands), len(self.out_shapes)
        relay_scratch = [pltpu.SemaphoreType.DMA((self.n_relay,))] * 2 if self.relay else []

        def body(*refs):
            ins, outs, sems = refs[:n_in], refs[n_in:n_in + n_out], refs[n_in + n_out:]
            mv = self.moves(ins, outs, sems[:3])
            mv.start()
            mv.wait_arrivals()
            if self.relay:
                passed = _Moves(sems[3], sems[4], None)
                self.relay(passed, ins, outs)
                passed.start()
                passed.wait()
            mv.wait_sent()

        return pl.pallas_call(
            body, name=name, in_specs=[ANY] * n_in, out_specs=[ANY] * n_out, out_shape=self.out_shapes,
            scratch_shapes=self.scratch() + relay_scratch, compiler_params=pltpu.CompilerParams(has_side_effects=True),
        )(*self.operands)


def _run_on_sequencer(exchange, name, collective_id):
    ins = [jax.new_ref(a, memory_space=pltpu.MemorySpace.HBM) for a in exchange.operands]
    outs = [jax.empty_ref(s, memory_space=pltpu.MemorySpace.HBM) for s in exchange.out_shapes]
    relay_scratch = [pltpu.SemaphoreType.DMA((exchange.n_relay,))] * 2 if exchange.relay else []

    def launch(*sems):
        x, y, c = _position()
        peers = [(_flip(x, fx), _flip(y, fy), c) for fx, fy in CHIP_FLIPS] + ([(x, y, 1 - c)] if exchange.relay else [])
        barrier = pltpu.get_barrier_semaphore()
        for peer in peers:
            pl.semaphore_signal(barrier, inc=1, device_id=peer, device_id_type=MESH)
        pl.semaphore_wait(barrier, len(peers))
        moves = exchange.moves(ins, outs, sems[:3])
        moves.start()
        moves.wait_arrivals()
        if exchange.relay:
            passed = _Moves(sems[3], sems[4], None)
            exchange.relay(passed, ins, outs)
            passed.start()
            passed.wait()
        moves.wait_sent()

    pl.kernel(launch, mesh=plsc.ScalarSubcoreMesh(axis_name="sequencer", num_cores=1), name=name,
              scratch_types=tuple(exchange.scratch() + relay_scratch),
              compiler_params=pltpu.CompilerParams(collective_id=collective_id))()
    return [o[...] for o in outs]


def _row_major_copy(a, name):
    r, c = a.shape
    tr = _row_tile(r)

    def body(a_ref, o_ref):
        o_ref[...] = a_ref[...]

    blk = pl.BlockSpec((tr, c), lambda i: (i, 0))
    return pl.pallas_call(body, name=name, grid=(r // tr,), in_specs=[blk], out_specs=blk,
                          out_shape=jax.ShapeDtypeStruct(a.shape, a.dtype), compiler_params=_params("parallel"))(a)


def _is_chip(x, y, chip):
    return jnp.logical_and(x == chip // 2, y == chip % 2)


def _gather_exchange(from_chip, from_all, split=()):
    n1, n2 = len(from_chip), len(from_all)

    def rows_of(t, c):
        half = from_chip[t][1].shape[0] // 2
        return pl.ds(c * half, half)

    def build(mv, ins, outs):
        x, y, c = _position()
        me = 2 * x + y
        for t, (chip, _) in enumerate(from_chip):
            mv.copy(ins[t], outs[t], cond=_is_chip(x, y, chip))
        for t in range(n2):
            mv.copy(ins[n1 + t], outs[n1 + t].at[me])
        for fx, fy in CHIP_FLIPS:
            px, py = _flip(x, fx), _flip(y, fy)
            peer = (px, py, c)
            for t, (chip, _) in enumerate(from_chip):
                part = rows_of(t, c) if t in split else slice(None)
                mv.send(ins[t].at[part], outs[t].at[part], peer, landing=outs[t].at[part],
                        send_if=_is_chip(x, y, chip), recv_if=_is_chip(px, py, chip))
            for t in range(n2):
                mv.send(ins[n1 + t], outs[n1 + t].at[me], peer, landing=outs[n1 + t].at[2 * px + py])

    def relay(mv, ins, outs):
        x, y, c = _position()
        for t in split:
            came = jnp.logical_not(_is_chip(x, y, from_chip[t][0]))
            mv.send(outs[t].at[rows_of(t, c)], outs[t].at[rows_of(t, c)], (x, y, 1 - c),
                    landing=outs[t].at[rows_of(t, 1 - c)], send_if=came, recv_if=came)

    arrays = [a for _, a in from_chip] + list(from_all)
    shapes = [jax.ShapeDtypeStruct(a.shape, a.dtype) for _, a in from_chip]
    shapes += [jax.ShapeDtypeStruct((N_CHIPS,) + a.shape, a.dtype) for a in from_all]
    return _Exchange(arrays, shapes, len(CHIP_FLIPS) * (n1 + n2), n1 + n2, build,
                     n_relay=len(split), relay=relay if split else None)


def _scatter_exchange(to_chip, to_all):
    n1, n2 = len(to_chip), len(to_all)

    def build(mv, ins, outs):
        x, y, c = _position()
        for f, (fx, fy) in enumerate(CHIP_FLIPS):
            px, py = _flip(x, fx), _flip(y, fy)
            peer = (px, py, c)
            for t, (chip, _) in enumerate(to_chip):
                mv.send(ins[t], outs[t].at[f], peer, landing=outs[t].at[f],
                        send_if=_is_chip(px, py, chip), recv_if=_is_chip(x, y, chip))
            for t in range(n2):
                mv.send(ins[n1 + t].at[2 * px + py], outs[n1 + t].at[f], peer, landing=outs[n1 + t].at[f])

    arrays = [a for _, a in to_chip] + list(to_all)
    shapes = [jax.ShapeDtypeStruct((len(CHIP_FLIPS),) + a.shape, a.dtype) for _, a in to_chip]
    shapes += [jax.ShapeDtypeStruct((len(CHIP_FLIPS),) + a.shape[1:], a.dtype) for a in to_all]
    return _Exchange(arrays, shapes, len(CHIP_FLIPS) * (n1 + n2), 0, build)


def _swap_sibling(tensors, name):
    n = len(tensors)

    def body(*refs):
        ins, outs = refs[:n], refs[n:2 * n]
        send_sems, recv_sems = refs[2 * n:]
        x, y, c = _position()
        copies = [pltpu.make_async_remote_copy(
            src_ref=ins[t], dst_ref=outs[t], send_sem=send_sems.at[t], recv_sem=recv_sems.at[t],
            device_id=(x, y, 1 - c), device_id_type=MESH) for t in range(n)]
        for cp in copies:
            cp.start()
        for cp in copies:
            cp.wait_recv()
        for cp in copies:
            cp.wait_send()

    return pl.pallas_call(
        body, name=name, in_specs=[ANY] * n, out_specs=[ANY] * n,
        out_shape=[jax.ShapeDtypeStruct(a.shape, a.dtype) for a in tensors],
        scratch_shapes=[pltpu.SemaphoreType.DMA((n,)), pltpu.SemaphoreType.DMA((n,))],
        compiler_params=pltpu.CompilerParams(has_side_effects=True),
    )(*tensors)


def _allreduce_small(slab):
    stages = 3

    def body(x_ref, o_ref, buf, send_sems, recv_sems):
        x, y, c = _position()
        peers = ((1 - x, y, c), (x, 1 - y, c), (x, y, 1 - c))
        o_ref[...] = x_ref[...]
        for k, peer in enumerate(peers):
            cp = pltpu.make_async_remote_copy(src_ref=o_ref, dst_ref=buf.at[k], send_sem=send_sems.at[k],
                                              recv_sem=recv_sems.at[k], device_id=peer, device_id_type=MESH)
            cp.start()
            cp.wait()
            o_ref[...] = o_ref[...] + buf[k]

    return pl.pallas_call(
        body, name="allreduce_small",
        in_specs=[pl.BlockSpec(memory_space=pltpu.VMEM)], out_specs=pl.BlockSpec(memory_space=pltpu.VMEM),
        out_shape=jax.ShapeDtypeStruct(slab.shape, slab.dtype),
        scratch_shapes=[pltpu.VMEM((stages,) + slab.shape, slab.dtype),
                        pltpu.SemaphoreType.DMA((stages,)), pltpu.SemaphoreType.DMA((stages,))],
        compiler_params=pltpu.CompilerParams(has_side_effects=True),
    )(slab)


def _row_tile(r):
    return min(r, 256)


def _sum4(stack, recv, me):
    _, r, c = stack.shape
    tr = _row_tile(r)

    def body(me_ref, own_ref, recv_ref, o_ref):
        o_ref[...] = (((own_ref[...] + recv_ref[0].astype(F32)) + recv_ref[1].astype(F32))
                      + recv_ref[2].astype(F32))

    return pl.pallas_call(
        body, name="sum_partials",
        grid_spec=pltpu.PrefetchScalarGridSpec(
            num_scalar_prefetch=1, grid=(r // tr,),
            in_specs=[pl.BlockSpec((None, tr, c), lambda i, me_ref: (me_ref[0], i, 0)),
                      pl.BlockSpec((len(CHIP_FLIPS), tr, c), lambda i, me_ref: (0, i, 0))],
            out_specs=pl.BlockSpec((tr, c), lambda i, me_ref: (i, 0))),
        out_shape=jax.ShapeDtypeStruct((r, c), F32), compiler_params=_params("parallel"),
    )(me, stack, recv)


def _sum_block(own, recv):
    r, c = own.shape
    tr = _row_tile(r)

    def body(own_ref, recv_ref, o_ref):
        o_ref[...] = (((own_ref[...] + recv_ref[0].astype(F32)) + recv_ref[1].astype(F32))
                      + recv_ref[2].astype(F32))

    return pl.pallas_call(
        body, name="sum_block", grid=(r // tr,),
        in_specs=[pl.BlockSpec((tr, c), lambda i: (i, 0)), pl.BlockSpec((len(CHIP_FLIPS), tr, c), lambda i: (0, i, 0))],
        out_specs=pl.BlockSpec((tr, c), lambda i: (i, 0)),
        out_shape=jax.ShapeDtypeStruct((r, c), F32), compiler_params=_params("parallel"),
    )(own, recv)


def _adamw_math(w, g, m, v):
    m = ADAM_B1 * m + (1.0 - ADAM_B1) * g
    v = ADAM_B2 * v + (1.0 - ADAM_B2) * (g * g)
    m_hat = m / (1.0 - ADAM_B1 ** ADAM_STEP)
    v_hat = v / (1.0 - ADAM_B2 ** ADAM_STEP)
    delta = -ADAM_LR * (m_hat / (jnp.sqrt(v_hat) + ADAM_EPS) + ADAM_WD * w)
    return delta, m, v


def _adamw(w, m, v, g_parts, name):
    r, c = w.shape
    tr = _row_tile(r)
    n = len(g_parts)

    def body(*refs):
        w_ref, m_ref, v_ref = refs[:3]
        g_refs = refs[3:3 + n]
        g_out, d_out, m_out, v_out = refs[3 + n:]
        g = g_refs[0][...]
        for ref in g_refs[1:]:
            g = g + ref[...]
        g_out[...] = g
        d_out[...], m_out[...], v_out[...] = _adamw_math(w_ref[...], g, m_ref[...], v_ref[...])

    blk = pl.BlockSpec((tr, c), lambda i: (i, 0))
    return pl.pallas_call(
        body, name=name, grid=(r // tr,), in_specs=[blk] * (3 + n), out_specs=[blk] * 4,
        out_shape=[jax.ShapeDtypeStruct((r, c), F32)] * 4, compiler_params=_params("parallel"),
    )(w, m, v, *g_parts)


def _adamw_small(total, w, m, v):
    sizes = [w[n].size for n in SMALL]
    flat = lambda d: [d[n].reshape(1, -1) for n in SMALL]
    k = len(SMALL)

    def body(*refs):
        total_ref, w_refs, m_refs, v_refs = refs[0], refs[1:1 + k], refs[1 + k:1 + 2 * k], refs[1 + 2 * k:1 + 3 * k]
        outs = refs[1 + 3 * k:]
        for i, size in enumerate(sizes):
            g = total_ref[i:i + 1, 0:size]
            outs[i][...] = g
            outs[k + i][...], outs[2 * k + i][...], outs[3 * k + i][...] = _adamw_math(
                w_refs[i][...], g, m_refs[i][...], v_refs[i][...])

    res = pl.pallas_call(
        body, name="adamw_small", out_shape=[jax.ShapeDtypeStruct((1, size), F32) for size in sizes] * 4,
        compiler_params=_params(),
    )(total, *flat(w), *flat(m), *flat(v))
    return [{n: res[j * k + i].reshape(w[n].shape) for i, n in enumerate(SMALL)} for j in range(4)]


SHARDED = ("w_in", "w_lora_up", "a_lora_up", "w_out_a", "w_out_b", "w_out")
ROW_SHARDED = ("w_out",)
SMALL = ("norm_g", "shift_mu", "w0", "a0", "k_k", "k_a", "r_k", "lnx_w", "lnx_b", "f_bias", "q_norm_g", "k_norm_g",
         "final_norm_g")
WEIGHTS = ("norm_g", "w_in", "shift_mu", "w_lora_up", "w0", "a_lora_up", "a0", "k_k", "k_a", "r_k", "lnx_w", "lnx_b",
           "f_bias", "q_norm_g", "k_norm_g", "w_out_a", "w_out_b", "w_out", "final_norm_g")
SLAB_ROWS = 16
SLAB_COLS = SEC


def _to_slab(named, extra=None):
    rows = [jnp.pad(named[n].reshape(1, -1), ((0, 0), (0, SLAB_COLS - named[n].size))) for n in SMALL]
    if extra is not None:
        rows.append(jnp.pad(extra.reshape(1, -1), ((0, 0), (0, SLAB_COLS - extra.size))))
    rows.append(jnp.zeros((SLAB_ROWS - len(rows), SLAB_COLS), F32))
    return jnp.concatenate(rows, axis=0)


def _by_chip(g, name):
    if name in ROW_SHARDED:
        return g.reshape(N_CHIPS, g.shape[0] // N_CHIPS, g.shape[1])
    r, c = g.shape
    return g.reshape(r, N_CHIPS, c // N_CHIPS).transpose(1, 0, 2)


def _from_chips(stack, name):
    if name in ROW_SHARDED:
        return stack.reshape(-1, stack.shape[2])
    _, r, c = stack.shape
    return stack.transpose(1, 0, 2).reshape(r, N_CHIPS * c)


def kernel(x, norm_g, w_in, shift_mu, w_lora_up, w0, a_lora_up, a0, k_k, k_a, r_k, lnx_w, lnx_b, f_bias, q_norm_g, k_norm_g, w_out_a, w_out_b, w_out, final_norm_g, loss_target, m_norm_g, m_w_in, m_shift_mu, m_w_lora_up, m_w0, m_a_lora_up, m_a0, m_k_k, m_k_a, m_r_k, m_lnx_w, m_lnx_b, m_f_bias, m_q_norm_g, m_k_norm_g, m_w_out_a, m_w_out_b, m_w_out, m_final_norm_g, v_norm_g, v_w_in, v_shift_mu, v_w_lora_up, v_w0, v_a_lora_up, v_a0, v_k_k, v_k_a, v_r_k, v_lnx_w, v_lnx_b, v_f_bias, v_q_norm_g, v_k_norm_g, v_w_out_a, v_w_out_b, v_w_out, v_final_norm_g):
    w = dict(norm_g=norm_g, w_in=w_in, shift_mu=shift_mu, w_lora_up=w_lora_up, w0=w0, a_lora_up=a_lora_up, a0=a0,
             k_k=k_k, k_a=k_a, r_k=r_k, lnx_w=lnx_w, lnx_b=lnx_b, f_bias=f_bias, q_norm_g=q_norm_g,
             k_norm_g=k_norm_g, w_out_a=w_out_a, w_out_b=w_out_b, w_out=w_out, final_norm_g=final_norm_g)
    m = dict(norm_g=m_norm_g, w_in=m_w_in, shift_mu=m_shift_mu, w_lora_up=m_w_lora_up, w0=m_w0,
             a_lora_up=m_a_lora_up, a0=m_a0, k_k=m_k_k, k_a=m_k_a, r_k=m_r_k, lnx_w=m_lnx_w, lnx_b=m_lnx_b,
             f_bias=m_f_bias, q_norm_g=m_q_norm_g, k_norm_g=m_k_norm_g, w_out_a=m_w_out_a, w_out_b=m_w_out_b,
             w_out=m_w_out, final_norm_g=m_final_norm_g)
    v = dict(norm_g=v_norm_g, w_in=v_w_in, shift_mu=v_shift_mu, w_lora_up=v_w_lora_up, w0=v_w0,
             a_lora_up=v_a_lora_up, a0=v_a0, k_k=v_k_k, k_a=v_k_a, r_k=v_r_k, lnx_w=v_lnx_w, lnx_b=v_lnx_b,
             f_bias=v_f_bias, q_norm_g=v_q_norm_g, k_norm_g=v_k_norm_g, w_out_a=v_w_out_a, w_out_b=v_w_out_b,
             w_out=v_w_out, final_norm_g=v_final_norm_g)
    shapes = {n: w[n].shape for n in WEIGHTS}

    shard = {n: w[n][0].astype(BF16) for n in SHARDED}
    late = ("w_out_a", "w_out_b", "w_out")
    loras = ("w_lora_up", "a_lora_up")
    w_in_head, w_in_tail = shard["w_in"][:, :A_TAIL], shard["w_in"][:, A_TAIL:]
    shard0, shard1_head, up_stack, aup_stack = _run_on_sequencer(_gather_exchange(
        [(0, shard["w_in"]), (1, w_in_head)], [shard[n] for n in loras], split=(0,)), "gather_early", 2)
    moments = (_row_major_copy(m["w_in"][0], "m_w_in_rows"), _row_major_copy(v["w_in"][0], "v_w_in_rows"))
    shard0, moments = lax.optimization_barrier((shard0, moments))
    w_a = jnp.concatenate([shard0, shard1_head], axis=1)

    def late_weights(arrived):
        shard1_tail, shard2, shard3 = arrived[:3]
        w_b = jnp.concatenate([shard1_tail, shard2[:, :B_TAIL], jnp.zeros((D_MODEL, SEC - FOX_REAL), BF16)], axis=1)
        w_g = jnp.concatenate([shard2[:, B_TAIL:], shard3], axis=1)
        return (w_b, w_g, *[_from_chips(s, n) for n, s in zip(late, arrived[3:])])

    own = {}

    def bwd_exchange(dw_b, dw_g, dwa, dwb, dwo):
        own["tail1"] = dw_b[:, :B_HEAD]
        own["block2"] = jnp.concatenate([dw_b[:, B_HEAD:FOX_REAL], dw_g[:, :G_HEAD]], axis=1)
        own["block3"] = dw_g[:, G_HEAD:]
        own.update({n: _by_chip(g, n) for n, g in zip(late, (dwa, dwb, dwo))})
        return _scatter_exchange([(1, own["tail1"].astype(BF16)), (2, own["block2"].astype(BF16)),
                                  (3, own["block3"].astype(BF16))], [own[n].astype(BF16) for n in late])

    def tail_exchange(dw_a, dw_up, da_up):
        own["block0"], own["head1"] = dw_a[:, :SHARD_COLS], dw_a[:, SHARD_COLS:]
        own.update({n: _by_chip(g, n) for n, g in zip(loras, (dw_up, da_up))})
        return _scatter_exchange([(0, own["block0"].astype(BF16)), (1, own["head1"].astype(BF16))],
                                 [own[n].astype(BF16) for n in loras])

    small = {n: w[n] for n in SMALL}
    loss_vec, grad_x, grads, sent, sent_last = _device_grads(
        x[0], loss_target[0], small, w_a, _from_chips(up_stack, "w_lora_up"), _from_chips(aup_stack, "a_lora_up"),
        late_weights, _gather_exchange([(1, w_in_tail), (2, shard["w_in"]), (3, shard["w_in"])], [shard[n] for n in late]),
        bwd_exchange, tail_exchange)

    total = _allreduce_small(_to_slab(grads, extra=loss_vec))
    loss = (0.5 / D_MODEL) * jnp.sum(total[len(SMALL)])
    out_g, out_d, out_m, out_v = _adamw_small(total, w, m, v)

    xpos, ypos, _ = _position()
    me = (2 * xpos + ypos).astype(jnp.int32).reshape(1)
    core_sum = {n: _sum4(own[n], r, me) for n, r in zip(late, sent[3:])}
    theirs = dict(zip(late, _swap_sibling([core_sum[n] for n in late], "swap_sibling_early")))
    sent_last, out_d["norm_g"], theirs = lax.optimization_barrier((sent_last, out_d["norm_g"], theirs))
    core_sum["w_in"] = lax.switch(me[0], [
        lambda: _sum_block(own["block0"], sent_last[0]),
        lambda: jnp.concatenate([_sum_block(own["head1"], sent_last[1]), _sum_block(own["tail1"], sent[0])], axis=1),
        lambda: _sum_block(own["block2"], sent[1]),
        lambda: _sum_block(own["block3"], sent[2])])
    core_sum.update({n: _sum4(own[n], r, me) for n, r in zip(loras, sent_last[2:])})
    rest = ("w_in",) + loras
    theirs.update(zip(rest, _swap_sibling([core_sum[n] for n in rest], "swap_sibling")))
    for n in SHARDED:
        m_n, v_n = moments if n == "w_in" else (m[n][0], v[n][0])
        g, d, m2, v2 = _adamw(w[n][0], m_n, v_n, [core_sum[n], theirs[n]], "adamw_" + n)
        out_g[n], out_d[n], out_m[n], out_v[n] = (a.reshape(shapes[n]) for a in (g, d, m2, v2))

    return (loss, grad_x.reshape(x.shape), *[out_g[n] for n in WEIGHTS], *[out_d[n] for n in WEIGHTS],
            *[out_m[n] for n in WEIGHTS], *[out_v[n] for n in WEIGHTS])
```

```python
import functools
import math

import jax
import jax.numpy as jnp
from jax import lax
from jax.experimental import pallas as pl
from jax.experimental.pallas import tpu as pltpu
from jax.experimental.pallas import tpu_sc as plsc

F32 = jnp.float32
BF16 = jnp.bfloat16

D_MODEL = 1024
D_HALF = 512
HEAD = 64
N_HEADS = 8
LORA = 64
RWKV_COLS = 2176
FOX_REAL = 2056
SEC = 2176
GATE_COLS = 2048
IN_COLS = 6280
N_CHIPS = 4
SHARD_COLS = IN_COLS // N_CHIPS
A_TAIL = RWKV_COLS - SHARD_COLS
B_HEAD = SHARD_COLS - A_TAIL
B_TAIL = FOX_REAL - B_HEAD
G_HEAD = SHARD_COLS - B_TAIL
RMS_EPS = 1e-6
LNX_EPS = 64e-5
ATT_SCALE = HEAD ** -0.5
NEG = -1e30

ADAM_LR = 0.001
ADAM_B1 = 0.9
ADAM_B2 = 0.999
ADAM_EPS = 1e-08
ADAM_WD = 0.01
ADAM_STEP = 10

LANES = 128
SUBLANES = 8
VMEM_LIMIT = 56 * 1024 * 1024
MESH = pl.DeviceIdType.MESH


def _params(*sem):
    return pltpu.CompilerParams(dimension_semantics=sem if sem else None, vmem_limit_bytes=VMEM_LIMIT)


def _sigmoid(x):
    return 1.0 / (1.0 + jnp.exp(-x))


def _log_sigmoid(x):
    return jnp.minimum(x, 0.0) - jnp.log(1.0 + jnp.exp(-jnp.abs(x)))


def _head_ones():
    r = lax.broadcasted_iota(jnp.int32, (LANES, LANES), 0) >> 6
    c = lax.broadcasted_iota(jnp.int32, (LANES, LANES), 1) >> 6
    return (r == c).astype(BF16)


def _split3(x):
    hi = x.astype(BF16)
    r1 = x - hi.astype(F32)
    mid = r1.astype(BF16)
    lo = (r1 - mid.astype(F32)).astype(BF16)
    return hi, mid, lo


def _exact_dot(x, ones_bf16, ones_first=False):
    out = None
    for piece in _split3(x):
        if ones_first:
            t = jnp.dot(ones_bf16, piece, preferred_element_type=F32)
        else:
            t = jnp.dot(piece, ones_bf16, preferred_element_type=F32)
        out = t if out is None else out + t
    return out


def _head_sum(x, bd):
    n = x.shape[1] // LANES
    parts = [_exact_dot(x[:, i * LANES:(i + 1) * LANES], bd) for i in range(n)]
    return parts[0] if n == 1 else jnp.concatenate(parts, axis=1)


def _dot_nt(a, b):
    return lax.dot_general(a, b, (((1,), (1,)), ((), ())), preferred_element_type=F32)


def _dot_tn(a, b):
    return lax.dot_general(a, b, (((0,), (0,)), ((), ())), preferred_element_type=F32)


def _colsum(x):
    return jnp.sum(x, axis=0, keepdims=True)


def _matmul_tn_acc(at, b, name, tk=512):
    m, k = at.shape
    n = b.shape[1]

    def body(a_ref, b_ref, o_ref):
        j = pl.program_id(0)

        @pl.when(j == 0)
        def _():
            o_ref[...] = jnp.zeros_like(o_ref)

        o_ref[...] += jnp.dot(a_ref[...], b_ref[...].astype(BF16), preferred_element_type=F32)

    return pl.pallas_call(
        body, name=name, grid=(k // tk,),
        in_specs=[pl.BlockSpec((m, tk), lambda j: (0, j)), pl.BlockSpec((tk, n), lambda j: (j, 0))],
        out_specs=pl.BlockSpec((m, n), lambda j: (0, 0)),
        out_shape=jax.ShapeDtypeStruct((m, n), F32), compiler_params=_params("arbitrary"),
    )(at, b)


def _inproj_bwd(du_a, du_b, du_g, w_a, w_b, w_g, x, dx2, g, exchange=None, tm=256):
    s, d = x.shape
    nb = s // tm

    def body(*refs):
        ((da_ref, db_ref, dg_ref, wa_ref, wb_ref, wg_ref, x_ref, dx2_ref, g_ref), (gx_ref, gg_ref), _,
         moves) = _split_refs(refs, 9, 2, exchange)
        i = pl.program_id(0)
        if moves:
            moves.start(also=(i == 0))

        @pl.when(i == 0)
        def _():
            gg_ref[...] = jnp.zeros_like(gg_ref)

        dh = _dot_nt(da_ref[...].astype(BF16), wa_ref[...])
        dh += _dot_nt(db_ref[...].astype(BF16), wb_ref[...])
        dh += _dot_nt(dg_ref[...].astype(BF16), wg_ref[...])
        xv = x_ref[...]
        r = lax.rsqrt(jnp.mean(xv * xv, axis=-1, keepdims=True) + RMS_EPS)
        xh = xv * r
        gg_ref[...] += _colsum(dh * xh)
        dxh = dh * g_ref[...]
        gx_ref[...] = dx2_ref[...] + r * (dxh - xh * jnp.mean(dxh * xh, axis=-1, keepdims=True))
        if moves:
            moves.wait(also=(i == nb - 1))

    row = lambda w: pl.BlockSpec((tm, w), lambda i: (i, 0))
    full = lambda a: pl.BlockSpec(a.shape, lambda i: (0, 0))
    ex_in = exchange.operands if exchange else []
    ex_out = exchange.out_shapes if exchange else []
    res = pl.pallas_call(
        body, name="inproj_bwd", grid=(nb,),
        in_specs=[row(SEC), row(SEC), row(GATE_COLS), full(w_a), full(w_b), full(w_g), row(d), row(d), full(g)]
                 + [ANY] * len(ex_in),
        out_specs=[row(d), pl.BlockSpec((1, d), lambda i: (0, 0))] + [ANY] * len(ex_out),
        out_shape=[jax.ShapeDtypeStruct((s, d), F32), jax.ShapeDtypeStruct((1, d), F32)] + ex_out,
        scratch_shapes=exchange.scratch() if exchange else [],
        compiler_params=_params("arbitrary"),
    )(du_a, du_b, du_g, w_a, w_b, w_g, x, dx2, g, *ex_in)
    return res[0], res[1], list(res[2:])


def _rwkv_elementwise(ua, prev_row, first, mu, wl, w0, a0, kkw, kaw, bd):
    tm = ua.shape[0]
    rows = lax.broadcasted_iota(jnp.int32, (tm, 1), 0)
    prev = jnp.where(first, jnp.zeros_like(prev_row), prev_row)
    shifted = jnp.where(rows == 0, prev, pltpu.roll(ua, 1, 0))
    delta = shifted - ua
    us = ua + delta * mu
    r = us[:, 0:512]
    k0 = us[:, 512:1024]
    v = us[:, 1024:1536]
    lo = us[:, 1536:1664]
    gate = us[:, 1664:2176]
    lane = lax.broadcasted_iota(jnp.int32, (1, LANES), 1)
    th = jnp.tanh(lo)
    lin = jnp.where(lane < LORA, th, lo)
    ll = jnp.dot(lin.astype(BF16), wl, preferred_element_type=F32)
    sz = _sigmoid(w0 + ll[:, :512])
    e = sz * math.exp(-0.5)
    dec = jnp.exp(-e)
    a = _sigmoid(a0 + ll[:, 512:])
    kk0 = k0 * kkw
    ss = _head_sum(kk0 * kk0, bd)
    nrm = jnp.maximum(jnp.sqrt(ss), 1e-12)
    kk = kk0 / nrm
    k = k0 * (1.0 + (a - 1.0) * kaw)
    return dict(delta=delta, us=us, r=r, k0=k0, v=v, lo=lo, gate=gate, th=th, lin=lin, sz=sz, e=e, dec=dec,
                a=a, kk0=kk0, ss=ss, nrm=nrm, kk=kk, k=k)


def _rwkv_front(x, g, w_a, mu, wl, w0, a0, kkw, kaw, tm=256):
    s, d = x.shape

    def body(x_ref, g_ref, wa_ref, mu_ref, wl_ref, w0_ref, a0_ref, kkw_ref, kaw_ref,
             h_ref, ua_ref, r_ref, w_ref, k_ref, v_ref, a_ref, b_ref, gate_ref, last_row):
        i = pl.program_id(0)
        xv = x_ref[...]
        h = (xv * lax.rsqrt(jnp.mean(xv * xv, axis=-1, keepdims=True) + RMS_EPS) * g_ref[...]).astype(BF16)
        h_ref[...] = h
        ua = jnp.dot(h, wa_ref[...], preferred_element_type=F32)
        ua_ref[...] = ua

        @pl.when(i == 0)
        def _():
            last_row[...] = jnp.zeros_like(last_row)

        f = _rwkv_elementwise(ua, last_row[...], i == 0, mu_ref[...], wl_ref[...], w0_ref[...],
                              a0_ref[...], kkw_ref[...], kaw_ref[...], _head_ones())
        last_row[...] = ua[tm - 1:tm, :]
        r_ref[...] = f["r"]
        w_ref[...] = f["dec"]
        k_ref[...] = f["k"]
        v_ref[...] = f["v"]
        a_ref[...] = -f["kk"]
        b_ref[...] = f["kk"] * f["a"]
        gate_ref[...] = f["gate"]

    vec = lambda w: pl.BlockSpec((1, w), lambda i: (0, 0))
    row = lambda w: pl.BlockSpec((tm, w), lambda i: (i, 0))
    return pl.pallas_call(
        body, name="rwkv_front", grid=(s // tm,),
        in_specs=[row(d), vec(d), pl.BlockSpec(w_a.shape, lambda i: (0, 0), pipeline_mode=pl.Buffered(1)),
                  vec(SEC), pl.BlockSpec((LANES, 2 * D_HALF), lambda i: (0, 0)),
                  vec(D_HALF), vec(D_HALF), vec(D_HALF), vec(D_HALF)],
        out_specs=[row(d), row(SEC)] + [row(D_HALF)] * 7,
        out_shape=[jax.ShapeDtypeStruct((s, d), BF16), jax.ShapeDtypeStruct((s, SEC), F32)]
                  + [jax.ShapeDtypeStruct((s, D_HALF), F32)] * 7,
        scratch_shapes=[pltpu.VMEM((1, SEC), F32)],
        compiler_params=_params("arbitrary"),
    )(x, g, w_a, mu, wl, w0, a0, kkw, kaw)


SCAN_TB = 128
N_PAIRS = 4


def _pair_sum(x, left):
    s_l = jnp.sum(jnp.where(left, x, 0.0), axis=1, keepdims=True)
    s_r = jnp.sum(jnp.where(left, 0.0, x), axis=1, keepdims=True)
    return jnp.where(left, s_l, s_r)


def _pair_dot(x, row_l, row_r, left):
    s_l = jnp.sum(x * row_l, axis=1, keepdims=True)
    s_r = jnp.sum(x * row_r, axis=1, keepdims=True)
    return jnp.where(left, s_l, s_r)


def _halves(rows8):
    lane = lax.broadcasted_iota(jnp.int32, rows8.shape, 1)
    keep_left = (lane & (LANES - 1)) < HEAD
    return jnp.where(keep_left, rows8, 0.0), jnp.where(keep_left, 0.0, rows8)


def _quad_consts():
    lane = lax.broadcasted_iota(jnp.int32, (HEAD, 2 * LANES), 1)
    rowi = lax.broadcasted_iota(jnp.int32, (HEAD, 2 * LANES), 0)
    diag2 = rowi == (lane & (HEAD - 1))
    r = lax.broadcasted_iota(jnp.int32, (2 * LANES, 2 * LANES), 0) >> 6
    c = lax.broadcasted_iota(jnp.int32, (2 * LANES, 2 * LANES), 1) >> 6
    return diag2, (r == c).astype(BF16)


def _rows_to_columns(x8, diag2, bd2):
    lhs = jnp.concatenate([jnp.where(diag2, x8[i:i + 1], 0.0).astype(BF16) for i in range(SUBLANES)], axis=0)
    return jnp.dot(lhs, bd2, preferred_element_type=F32)


def _diag_rows(qtile, diag2, bd2, sub_row2):
    res = jnp.dot(qtile, bd2, preferred_element_type=F32)
    out = jnp.zeros((SUBLANES, 2 * LANES), F32)
    for i in range(SUBLANES):
        out = jnp.where(sub_row2 == i, _colsum(jnp.where(diag2, res[i * HEAD:(i + 1) * HEAD], 0.0)), out)
    return out


def _store_tile(qbuf, slot, p, i, x):
    qbuf[slot, p // 2, i * HEAD:(i + 1) * HEAD, (p % 2) * LANES:(p % 2 + 1) * LANES] = x.astype(BF16)


def _left_half():
    return lax.broadcasted_iota(jnp.int32, (HEAD, LANES), 1) < HEAD


def _split_refs(refs, n_rows, n_out, exchange):
    n_in = len(exchange.operands) if exchange else 0
    n_ex_out = len(exchange.out_shapes) if exchange else 0
    refs = list(refs)
    rows, refs = refs[:n_rows], refs[n_rows:]
    ex_in, refs = refs[:n_in], refs[n_in:]
    outs, refs = refs[:n_out], refs[n_out:]
    ex_out, refs = refs[:n_ex_out], refs[n_ex_out:]
    scratch, sems = (refs[:-3], refs[-3:]) if exchange else (refs, None)
    moves = exchange.moves(ex_in, ex_out, sems) if exchange else None
    return rows, outs, scratch, moves


def _wkv_fwd(r, w, k, a, b, v, exchange=None):
    s = r.shape[0]
    tb = SCAN_TB
    nb = s // tb

    def body(*refs):
        (r_ref, w_ref, k_ref, a_ref, b_ref, v_ref), (y_ref, st_ref), (state, vbuf, qbuf), moves = _split_refs(
            refs, 6, 2, exchange)
        g = pl.program_id(0)
        if moves:
            moves.start(also=(g == 0))

        @pl.when(g == 0)
        def _():
            state[...] = jnp.zeros_like(state)
            qbuf[...] = jnp.zeros_like(qbuf)

        left = _left_half()
        diag2, bd2 = _quad_consts()
        sub_row2 = lax.broadcasted_iota(jnp.int32, (SUBLANES, 2 * LANES), 0)
        groups = tb // SUBLANES
        quads = [slice(g2 * 2 * LANES, (g2 + 1) * 2 * LANES) for g2 in range(2)]

        def rows_of(q):
            return pl.ds(pl.multiple_of(q * SUBLANES, SUBLANES), SUBLANES)

        def v_tiles(q, slot):
            v8 = v_ref[rows_of(q), :]
            for g2 in range(2):
                vbuf[slot, g2] = _rows_to_columns(v8[:, quads[g2]], diag2, bd2)

        def chain(q, slot):
            rows8 = rows_of(q)
            a8, w8, b8, k8, r8 = (x[rows8, :] for x in (a_ref, w_ref, b_ref, k_ref, r_ref))
            pairs = [slice(p * LANES, (p + 1) * LANES) for p in range(N_PAIRS)]
            a_next = pltpu.roll(a8, SUBLANES - 1, 0)
            (a8_l, a8_r), (wa8_l, wa8_r) = _halves(a8), _halves(w8 * a_next)
            ba8 =jnp.concatenate([_pair_sum(b8[:, pr] * a_next[:, pr], left[0:SUBLANES]) for pr in pairs], axis=1)
            ka8 = jnp.concatenate([_pair_sum(k8[:, pr] * a_next[:, pr], left[0:SUBLANES]) for pr in pairs], axis=1)
            sp = [state[p] for p in range(N_PAIRS)]
            for i in range(0, SUBLANES, 2):
                r0, r1 = slice(i, i + 1), slice(i + 1, i + 2)
                sums = [(_pair_dot(sp[p], a8_l[r0, pairs[p]], a8_r[r0, pairs[p]], left),
                         _pair_dot(sp[p], wa8_l[r0, pairs[p]], wa8_r[r0, pairs[p]], left)) for p in range(N_PAIRS)]
                sa0, sa1 = [s[0] for s in sums], [s[1] for s in sums]
                for p in range(N_PAIRS):
                    pr = pairs[p]
                    inner = slice((p % 2) * LANES, (p % 2 + 1) * LANES)
                    vt0 = vbuf[slot, p // 2, i * HEAD:(i + 1) * HEAD, inner]
                    vt1 = vbuf[slot, p // 2, (i + 1) * HEAD:(i + 2) * HEAD, inner]
                    sa_next = sa1[p] + sa0[p] * ba8[r0, pr] + vt0 * ka8[r0, pr]
                    s1 = sp[p] * w8[r0, pr] + sa0[p] * b8[r0, pr] + vt0 * k8[r0, pr]
                    st_ref[q * SUBLANES + i, p] = s1
                    _store_tile(qbuf, slot, p, i, s1 * r8[r0, pr])
                    s2 = s1 * w8[r1, pr] + sa_next * b8[r1, pr] + vt1 * k8[r1, pr]
                    st_ref[q * SUBLANES + i + 1, p] = s2
                    _store_tile(qbuf, slot, p, i + 1, s2 * r8[r1, pr])
                    sp[p] = s2
            for p in range(N_PAIRS):
                state[p] = sp[p]

        def y_rows(q, slot):
            for g2 in range(2):
                y_ref[rows_of(q), quads[g2]] = _diag_rows(qbuf[slot, g2], diag2, bd2, sub_row2)

        v_tiles(0, 0)

        def two_groups(j, carry):
            q0 = 2 * j
            v_tiles(q0 + 1, 1)
            chain(q0, 0)
            y_rows(jnp.maximum(q0 - 1, 0), 1)
            v_tiles(jnp.minimum(q0 + 2, groups - 1), 0)
            chain(q0 + 1, 1)
            y_rows(q0, 0)
            return carry

        lax.fori_loop(0, groups // 2, two_groups, 0)
        y_rows(groups - 1, 1)
        if moves:
            moves.wait(also=(g == nb - 1))

    rows = pl.BlockSpec((tb, D_HALF), lambda g: (g, 0))
    ex_in = exchange.operands if exchange else []
    ex_out = exchange.out_shapes if exchange else []
    res = pl.pallas_call(
        body, name="wkv_fwd", grid=(nb,),
        in_specs=[rows] * 6 + [ANY] * len(ex_in),
        out_specs=[rows, pl.BlockSpec((tb, N_PAIRS, HEAD, LANES), lambda g: (g, 0, 0, 0))] + [ANY] * len(ex_out),
        out_shape=[jax.ShapeDtypeStruct((s, D_HALF), F32),
                   jax.ShapeDtypeStruct((s, N_PAIRS, HEAD, LANES), F32)] + ex_out,
        scratch_shapes=[pltpu.VMEM((N_PAIRS, HEAD, LANES), F32),
                        pltpu.VMEM((2, 2, SUBLANES * HEAD, 2 * LANES), F32),
                        pltpu.VMEM((2, 2, SUBLANES * HEAD, 2 * LANES), BF16)]
                       + (exchange.scratch() if exchange else []),
        compiler_params=_params("arbitrary"),
    )(r, w, k, a, b, v, *ex_in)
    return res[0], res[1], list(res[2:])


def _wkv_bwd(r, w, k, a, b, v, dy, st, exchange=None):
    s = r.shape[0]
    tb = SCAN_TB
    nb = s // tb

    def body(*refs):
        ((r_ref, w_ref, k_ref, a_ref, b_ref, v_ref, dy_ref, st_ref, before_ref),
         (dr_ref, dw_ref, dk_ref, dv_ref, da_ref, db_ref), (dstate, vbuf, qbuf, sbuf),
         moves) = _split_refs(refs, 9, 6, exchange)
        g = pl.program_id(0)
        first_block = g == nb - 1
        if moves:
            moves.start(also=(g == 0))

        @pl.when(g == 0)
        def _():
            dstate[...] = jnp.zeros_like(dstate)
            qbuf[...] = jnp.zeros_like(qbuf)

        left = _left_half()
        diag2, bd2 = _quad_consts()
        sub_row = lax.broadcasted_iota(jnp.int32, (SUBLANES, LANES), 0)
        sub_row2 = lax.broadcasted_iota(jnp.int32, (SUBLANES, 2 * LANES), 0)
        groups = tb // SUBLANES
        quads = [slice(g2 * 2 * LANES, (g2 + 1) * 2 * LANES) for g2 in range(2)]
        row_refs = (dr_ref, dw_ref, dk_ref, da_ref, db_ref)

        def rows_of(q):
            return pl.ds(pl.multiple_of(q * SUBLANES, SUBLANES), SUBLANES)

        def state_before(q, i, p):
            if i > 0:
                return st_ref[q * SUBLANES + i - 1, p]
            return jnp.where(q == 0, jnp.where(first_block, 0.0, before_ref[0, p]),
                             st_ref[jnp.maximum(q * SUBLANES - 1, 0), p])

        def column_tiles(q, slot):
            rows8 = rows_of(q)
            for kind, ref in enumerate((v_ref, dy_ref)):
                x8 = ref[rows8, :]
                for g2 in range(2):
                    vbuf[slot, kind, g2] = _rows_to_columns(x8[:, quads[g2]], diag2, bd2)
            a8 = a_ref[rows8, :]
            for i in range(SUBLANES):
                for p in range(N_PAIRS):
                    _store_tile(sbuf, 0, p, i, state_before(q, i, p) * a8[i:i + 1, p * LANES:(p + 1) * LANES])
            for g2 in range(2):
                vbuf[slot, 2, g2] = jnp.dot(sbuf[0, g2], bd2, preferred_element_type=F32)

        def chain(q, slot):
            rows8 = rows_of(q)
            a8, w8, b8, k8, r8 = (x[rows8, :] for x in (a_ref, w_ref, b_ref, k_ref, r_ref))
            b8_l, b8_r = _halves(b8)
            dsp = [dstate[p] for p in range(N_PAIRS)]
            outs = [[jnp.zeros((SUBLANES, LANES), F32) for _ in row_refs] for _ in range(N_PAIRS)]
            after = [st_ref[q * SUBLANES + SUBLANES - 1, p] for p in range(N_PAIRS)]
            for i in reversed(range(SUBLANES)):
                row = slice(i, i + 1)
                pl_ = [slice(p * LANES, (p + 1) * LANES) for p in range(N_PAIRS)]
                tile = [(p // 2, slice(i * HEAD, (i + 1) * HEAD), slice((p % 2) * LANES, (p % 2 + 1) * LANES))
                        for p in range(N_PAIRS)]
                sp = [state_before(q, i, p) for p in range(N_PAIRS)]
                dyt = [vbuf[(slot, 1) + tile[p]] for p in range(N_PAIRS)]
                ds = [dsp[p] + dyt[p] * r8[row, pl_[p]] for p in range(N_PAIRS)]
                dsa = [_pair_dot(ds[p], b8_l[row, pl_[p]], b8_r[row, pl_[p]], left) for p in range(N_PAIRS)]
                sa = [vbuf[(slot, 2) + tile[p]] for p in range(N_PAIRS)]
                for p in range(N_PAIRS):
                    ar, wr, br, kr = (x[row, pl_[p]] for x in (a8, w8, b8, k8))
                    vt = vbuf[(slot, 0) + tile[p]]
                    dsp[p] = ds[p] * wr + dsa[p] * ar
                    new = (_colsum(after[p] * dyt[p]), _colsum(ds[p] * sp[p]), _colsum(ds[p] * vt),
                           _colsum(sp[p] * dsa[p]), _colsum(ds[p] * sa[p]))
                    outs[p] = [jnp.where(sub_row == i, n, o) for n, o in zip(new, outs[p])]
                    _store_tile(qbuf, slot, p, i, ds[p] * kr)
                after = sp
            for p in range(N_PAIRS):
                dstate[p] = dsp[p]
                for ref, o in zip(row_refs, outs[p]):
                    ref[rows8, p * LANES:(p + 1) * LANES] = o

        def dv_rows(q, slot):
            for g2 in range(2):
                dv_ref[rows_of(q), quads[g2]] = _diag_rows(qbuf[slot, g2], diag2, bd2, sub_row2)

        column_tiles(groups - 1, 0)

        def two_groups(j, carry):
            q0 = groups - 1 - 2 * j
            column_tiles(q0 - 1, 1)
            chain(q0, 0)
            dv_rows(jnp.minimum(q0 + 1, groups - 1), 1)
            column_tiles(jnp.maximum(q0 - 2, 0), 0)
            chain(q0 - 1, 1)
            dv_rows(q0, 0)
            return carry

        lax.fori_loop(0, groups // 2, two_groups, 0)
        dv_rows(0, 1)
        if moves:
            moves.wait(also=(g == nb - 1))

    rows = pl.BlockSpec((tb, D_HALF), lambda g: (nb - 1 - g, 0))
    ex_in = exchange.operands if exchange else []
    ex_out = exchange.out_shapes if exchange else []
    res = pl.pallas_call(
        body, name="wkv_bwd", grid=(nb,),
        in_specs=[rows] * 7 + [pl.BlockSpec((tb, N_PAIRS, HEAD, LANES), lambda g: (nb - 1 - g, 0, 0, 0)),
                               pl.BlockSpec((1, N_PAIRS, HEAD, LANES),
                                            lambda g: (jnp.maximum((nb - 1 - g) * tb - 1, 0), 0, 0, 0))]
                 + [ANY] * len(ex_in),
        out_specs=[rows] * 6 + [ANY] * len(ex_out),
        out_shape=[jax.ShapeDtypeStruct((s, D_HALF), F32)] * 6 + ex_out,
        scratch_shapes=[pltpu.VMEM((N_PAIRS, HEAD, LANES), F32),
                        pltpu.VMEM((2, 3, 2, SUBLANES * HEAD, 2 * LANES), F32),
                        pltpu.VMEM((2, 2, SUBLANES * HEAD, 2 * LANES), BF16),
                        pltpu.VMEM((1, 2, SUBLANES * HEAD, 2 * LANES), BF16)]
                       + (exchange.scratch() if exchange else []),
        compiler_params=_params("arbitrary"),
    )(r, w, k, a, b, v, dy, st, st, *ex_in)
    return list(res[:6]), list(res[6:])


def _rwkv_post_math(y, r, k, v, gate, lw, lb, rk, bd):
    mean = _head_sum(y, bd) * (1.0 / HEAD)
    yc = y - mean
    var = _head_sum(yc * yc, bd) * (1.0 / HEAD)
    rstd = lax.rsqrt(var + LNX_EPS)
    yn = yc * rstd
    rkk = _head_sum(r * k * rk, bd)
    sg = _sigmoid(gate)
    pre = yn * lw + lb + rkk * v
    return yn, rstd, rkk, sg, pre


def _rwkv_prep_bwd(u_a, h_t, grads, mu, wl, w0, a0, kkw, kaw, tm=256):
    s = u_a.shape[0]
    nb = s // tm
    d = h_t.shape[0]

    def body(ua_ref, prev_ref, ht_ref, drs_ref, dws_ref, dks_ref, dvs_ref, das_ref, dbs_ref, drb_ref, dkb_ref, dvb_ref,
             dgt_ref, mu_ref, wl_ref, w0_ref, a0_ref, kkw_ref, kaw_ref,
             du_ref, dwa_ref, dmu_ref, dwl_ref, dw0_ref, da0_ref, dkkw_ref, dkaw_ref, carry):
        i = pl.program_id(0)

        @pl.when(i == 0)
        def _():
            carry[...] = jnp.zeros_like(carry)
            for ref in (dwa_ref, dmu_ref, dwl_ref, dw0_ref, da0_ref, dkkw_ref, dkaw_ref):
                ref[...] = jnp.zeros_like(ref)

        bd = _head_ones()
        mu_v, wl_v, kkw_v, kaw_v = mu_ref[...], wl_ref[...], kkw_ref[...], kaw_ref[...]
        f = _rwkv_elementwise(ua_ref[...], prev_ref[7:8, :], i == nb - 1, mu_v, wl_v, w0_ref[...],
                              a0_ref[...], kkw_v, kaw_v, bd)
        a, kk, k0 = f["a"], f["kk"], f["k0"]
        dk = dks_ref[...] + dkb_ref[...]
        dbs = dbs_ref[...]
        dkk = dbs * a - das_ref[...]
        da = dbs * kk + dk * k0 * kaw_v
        dk0 = dk * (1.0 + (a - 1.0) * kaw_v)
        dkaw_ref[...] += _colsum(dk * k0 * (a - 1.0))
        inv = 1.0 / f["nrm"]
        proj = _head_sum(dkk * kk, bd)
        dkk0 = jnp.where(f["ss"] > 1e-24, (dkk - kk * proj) * inv, dkk * inv)
        dk0 = dk0 + dkk0 * kkw_v
        dkkw_ref[...] += _colsum(dkk0 * k0)
        dza = da * a * (1.0 - a)
        da0_ref[...] += _colsum(dza)
        dz = -dws_ref[...] * f["dec"] * f["e"] * (1.0 - f["sz"])
        dw0_ref[...] += _colsum(dz)
        dll = jnp.concatenate([dz, dza], axis=1).astype(BF16)
        dwl_ref[...] += _dot_tn(f["lin"].astype(BF16), dll)
        dlin = _dot_nt(dll, wl_v)
        lane = lax.broadcasted_iota(jnp.int32, (1, LANES), 1)
        th = f["th"]
        dlo = jnp.where(lane < LORA, dlin * (1.0 - th * th), dlin)
        dus = jnp.concatenate([drs_ref[...] + drb_ref[...], dk0, dvs_ref[...] + dvb_ref[...], dlo, dgt_ref[...]],
                              axis=1)
        dmu_ref[...] += _colsum(dus * f["delta"])
        g1 = dus * mu_v
        rows = lax.broadcasted_iota(jnp.int32, (tm, 1), 0)
        up = jnp.where(rows == tm - 1, carry[...], pltpu.roll(g1, tm - 1, 0))
        dua = dus - g1 + up
        du_ref[...] = dua
        dwa_ref[...] += jnp.dot(ht_ref[...], dua.astype(BF16), preferred_element_type=F32)
        carry[...] = g1[0:1, :]

    rev = lambda w: pl.BlockSpec((tm, w), lambda i: (nb - 1 - i, 0))
    vec = lambda w: pl.BlockSpec((1, w), lambda i: (0, 0))
    wl_spec = pl.BlockSpec((LANES, 2 * D_HALF), lambda i: (0, 0))
    return pl.pallas_call(
        body, name="rwkv_prep_bwd", grid=(nb,),
        in_specs=[rev(SEC), pl.BlockSpec((8, SEC), lambda i: (jnp.maximum((nb - 1 - i) * (tm // 8) - 1, 0), 0)),
                  pl.BlockSpec((d, tm), lambda i: (0, nb - 1 - i))]
                 + [rev(D_HALF)] * 10 + [vec(SEC), wl_spec] + [vec(D_HALF)] * 4,
        out_specs=[rev(SEC), pl.BlockSpec((d, SEC), lambda i: (0, 0)), vec(SEC), wl_spec] + [vec(D_HALF)] * 4,
        out_shape=[jax.ShapeDtypeStruct((s, SEC), F32), jax.ShapeDtypeStruct((d, SEC), F32),
                   jax.ShapeDtypeStruct((1, SEC), F32),
                   jax.ShapeDtypeStruct((LANES, 2 * D_HALF), F32)] + [jax.ShapeDtypeStruct((1, D_HALF), F32)] * 4,
        scratch_shapes=[pltpu.VMEM((1, SEC), F32)],
        compiler_params=_params("arbitrary"),
    )(u_a, u_a, h_t, *grads, mu, wl, w0, a0, kkw, kaw)


def _tri(tm, lower):
    r = lax.broadcasted_iota(jnp.int32, (tm, tm), 0)
    c = lax.broadcasted_iota(jnp.int32, (tm, tm), 1)
    return ((r >= c) if lower else (r <= c)).astype(BF16)


def _head_rms(x, g, bd):
    rinv = lax.rsqrt(_head_sum(x * x, bd) * (1.0 / HEAD) + RMS_EPS)
    xh = x * rinv
    return xh, rinv, xh * g


def _fox_front(h, w_b, fb, qg, kg, tm=256):
    s, d = h.shape

    def body(h_ref, wb_ref, fb_ref, qg_ref, kg_ref, ub_ref, q_ref, k_ref, v_ref, cc_ref, cr_ref, carry):
        i = pl.program_id(0)

        @pl.when(i == 0)
        def _():
            carry[...] = jnp.zeros_like(carry)

        ub_ref[...] = jnp.dot(h_ref[...], wb_ref[...], preferred_element_type=F32)
        bd = _head_ones()
        _, _, qn = _head_rms(ub_ref[:, 0:512], qg_ref[...], bd)
        _, _, kn = _head_rms(ub_ref[:, 512:1024], kg_ref[...], bd)
        q_ref[...] = (qn * ATT_SCALE).astype(BF16)
        k_ref[...] = kn.astype(BF16)
        v_ref[...] = ub_ref[:, 1024:1536].astype(BF16)
        lane = lax.broadcasted_iota(jnp.int32, (1, LANES), 1)
        logf = jnp.where(lane < N_HEADS, _log_sigmoid(ub_ref[:, 2048:2176] + fb_ref[...]), 0.0)
        cum = _exact_dot(logf, _tri(tm, True), ones_first=True) + carry[...]
        for h in range(N_HEADS):
            cc_ref[h] = jnp.broadcast_to(cum[:, h:h + 1], (tm, LANES))
        cr_ref[...] = jnp.transpose(cum)[0:N_HEADS, :]
        carry[...] = cum[tm - 1:tm, :]

    blk = pl.BlockSpec((tm, D_HALF), lambda i: (i, 0))
    return pl.pallas_call(
        body, name="fox_front", grid=(s // tm,),
        in_specs=[pl.BlockSpec((tm, d), lambda i: (i, 0)),
                  pl.BlockSpec(w_b.shape, lambda i: (0, 0), pipeline_mode=pl.Buffered(1)),
                  pl.BlockSpec((1, LANES), lambda i: (0, 0)),
                  pl.BlockSpec((1, D_HALF), lambda i: (0, 0)), pl.BlockSpec((1, D_HALF), lambda i: (0, 0))],
        out_specs=[pl.BlockSpec((tm, SEC), lambda i: (i, 0)), blk, blk, blk,
                   pl.BlockSpec((N_HEADS, tm, LANES), lambda i: (0, i, 0)), pl.BlockSpec((N_HEADS, tm), lambda i: (0, i))],
        out_shape=[jax.ShapeDtypeStruct((s, SEC), F32)] + [jax.ShapeDtypeStruct((s, D_HALF), BF16)] * 3
                  + [jax.ShapeDtypeStruct((N_HEADS, s, LANES), F32), jax.ShapeDtypeStruct((N_HEADS, s), F32)],
        scratch_shapes=[pltpu.VMEM((1, LANES), F32)],
        compiler_params=_params("arbitrary"),
    )(h, w_b, fb, qg, kg)


ATT_T = 256


def _attn_fwd(q, k, v, cc, cr):
    s = q.shape[0]
    t = ATT_T
    nblk = s // t

    def body(q_ref, k_ref, v_ref, cc_ref, cr_ref, o_ref, lse_ref, m_sc, l_sc, acc_sc):
        i = pl.program_id(0)
        j = pl.program_id(1)

        @pl.when(j == 0)
        def _():
            m_sc[...] = jnp.full_like(m_sc, NEG)
            l_sc[...] = jnp.zeros_like(l_sc)
            acc_sc[...] = jnp.zeros_like(acc_sc)

        def tile(on_diagonal):
            causal = _causal_tile(t) if on_diagonal else None
            left = lax.broadcasted_iota(jnp.int32, (1, LANES), 1) < HEAD
            for p in range(N_PAIRS):
                lanes = slice(p * LANES, (p + 1) * LANES)
                q2, k2, v2 = q_ref[:, lanes], k_ref[:, lanes], v_ref[:, lanes]
                acc2 = acc_sc[:, lanes]
                for e in range(2):
                    h = 2 * p + e
                    msk = left if e == 0 else jnp.logical_not(left)
                    sc = _dot_nt(jnp.where(msk, q2, jnp.zeros_like(q2)), k2)
                    sc = sc + (_wide(cc_ref[h]) - cr_ref[h:h + 1, :])
                    if on_diagonal:
                        sc = jnp.where(causal, sc, NEG)
                    m_prev = m_sc[h]
                    m_new = jnp.maximum(m_prev, jnp.max(sc, axis=1, keepdims=True))
                    alpha = jnp.exp(m_prev - m_new)
                    pm = jnp.exp(sc - _wide(m_new))
                    l_sc[h] = alpha * l_sc[h] + jnp.sum(pm, axis=1, keepdims=True)
                    m_sc[h] = m_new
                    pv = jnp.dot(pm.astype(BF16), v2, preferred_element_type=F32)
                    acc2 = jnp.where(msk, alpha * acc2 + pv, acc2)
                acc_sc[:, lanes] = acc2

        pl.when(j < i)(functools.partial(tile, False))
        pl.when(j == i)(functools.partial(tile, True))

        @pl.when(j == i)
        def _():
            left = lax.broadcasted_iota(jnp.int32, (1, LANES), 1) < HEAD
            for p in range(N_PAIRS):
                lanes = slice(p * LANES, (p + 1) * LANES)
                inv = jnp.where(left, 1.0 / l_sc[2 * p], 1.0 / l_sc[2 * p + 1])
                o_ref[:, lanes] = acc_sc[:, lanes] * inv
            for h in range(N_HEADS):
                lse_ref[h] = m_sc[h] + jnp.log(l_sc[h])

    qblk = pl.BlockSpec((t, D_HALF), lambda i, j: (i, 0))
    kblk = pl.BlockSpec((t, D_HALF), lambda i, j: (jnp.minimum(i, j), 0))
    return pl.pallas_call(
        body, name="fox_attn_fwd", grid=(nblk, nblk),
        in_specs=[qblk, kblk, kblk, pl.BlockSpec((N_HEADS, t, LANES), lambda i, j: (0, i, 0)),
                  pl.BlockSpec((N_HEADS, t), lambda i, j: (0, jnp.minimum(i, j)))],
        out_specs=[qblk, pl.BlockSpec((N_HEADS, t, LANES), lambda i, j: (0, i, 0))],
        out_shape=[jax.ShapeDtypeStruct((s, D_HALF), F32), jax.ShapeDtypeStruct((N_HEADS, s, LANES), F32)],
        scratch_shapes=[pltpu.VMEM((N_HEADS, t, LANES), F32), pltpu.VMEM((N_HEADS, t, LANES), F32),
                        pltpu.VMEM((t, D_HALF), F32)],
        compiler_params=_params("parallel", "arbitrary"),
    )(q, k, v, cc, cr)


def _causal_tile(t):
    return lax.broadcasted_iota(jnp.int32, (t, t), 0) >= lax.broadcasted_iota(jnp.int32, (t, t), 1)


def _wide(x):
    return jnp.concatenate([x, x], axis=1)


def _attn_probs(q2, k2, v2, do2, msk, causal, bias, lse_rows):
    zero = jnp.zeros_like(q2)
    qh = jnp.where(msk, q2, zero)
    doh = jnp.where(msk, do2, zero)
    sc = _dot_nt(qh, k2) + bias
    if causal is not None:
        sc = jnp.where(causal, sc, NEG)
    pm = jnp.exp(sc - _wide(lse_rows))
    dp = _dot_nt(doh, v2)
    return qh, doh, pm, dp


def _attn_bwd_rowdot(q, k, v, do, lse, cc, cr):
    s = q.shape[0]
    t = ATT_T
    nblk = s // t

    def body(q_ref, k_ref, v_ref, do_ref, lse_ref, cc_ref, cr_ref, dd_ref, acc):
        i = pl.program_id(0)
        j = pl.program_id(1)

        @pl.when(j == 0)
        def _():
            acc[...] = jnp.zeros_like(acc)

        def tile(on_diagonal):
            causal = _causal_tile(t) if on_diagonal else None
            left = lax.broadcasted_iota(jnp.int32, (1, LANES), 1) < HEAD
            for p in range(N_PAIRS):
                lanes = slice(p * LANES, (p + 1) * LANES)
                q2, k2, v2, do2 = q_ref[:, lanes], k_ref[:, lanes], v_ref[:, lanes], do_ref[:, lanes]
                for e in range(2):
                    h = 2 * p + e
                    msk = left if e == 0 else jnp.logical_not(left)
                    bias = _wide(cc_ref[h]) - cr_ref[h:h + 1, :]
                    _, _, pm, dp = _attn_probs(q2, k2, v2, do2, msk, causal, bias, lse_ref[h])
                    acc[h] += jnp.sum(pm * dp, axis=1, keepdims=True)

        pl.when(j < i)(functools.partial(tile, False))
        pl.when(j == i)(functools.partial(tile, True))

        @pl.when(j == i)
        def _():
            dd_ref[...] = acc[...]

    qblk = pl.BlockSpec((t, D_HALF), lambda i, j: (i, 0))
    qcol = pl.BlockSpec((N_HEADS, t, LANES), lambda i, j: (0, i, 0))
    kblk = pl.BlockSpec((t, D_HALF), lambda i, j: (jnp.minimum(i, j), 0))
    return pl.pallas_call(
        body, name="fox_attn_rowdot", grid=(nblk, nblk),
        in_specs=[qblk, kblk, kblk, qblk, qcol, qcol, pl.BlockSpec((N_HEADS, t), lambda i, j: (0, jnp.minimum(i, j)))],
        out_specs=qcol, out_shape=jax.ShapeDtypeStruct((N_HEADS, s, LANES), F32),
        scratch_shapes=[pltpu.VMEM((N_HEADS, t, LANES), F32)],
        compiler_params=_params("parallel", "arbitrary"),
    )(q, k, v, do, lse, cc, cr)


def _attn_bwd(q, k, v, do, lse, dd, cc, cr):
    s = q.shape[0]
    t = ATT_T
    nblk = s // t

    def body(q_ref, k_ref, v_ref, do_ref, lse_ref, dd_ref, cc_ref, cr_ref,
             dq_ref, dk_ref, dv_ref, dcr_ref, dk_sc, dv_sc, dcr_sc):
        j = pl.program_id(0)
        i = pl.program_id(1)

        @pl.when(jnp.logical_and(j == 0, i == 0))
        def _():
            dq_ref[...] = jnp.zeros_like(dq_ref)

        @pl.when(i == 0)
        def _():
            dk_sc[...] = jnp.zeros_like(dk_sc)
            dv_sc[...] = jnp.zeros_like(dv_sc)
            dcr_sc[...] = jnp.zeros_like(dcr_sc)

        def tile(on_diagonal):
            causal = _causal_tile(t) if on_diagonal else None
            left = lax.broadcasted_iota(jnp.int32, (1, LANES), 1) < HEAD
            qrows = pl.ds(pl.multiple_of(i * t, t), t)
            for p in range(N_PAIRS):
                lanes = slice(p * LANES, (p + 1) * LANES)
                q2, k2, v2, do2 = q_ref[:, lanes], k_ref[:, lanes], v_ref[:, lanes], do_ref[:, lanes]
                zero = jnp.zeros_like(q2)
                dq2 = jnp.zeros((t, LANES), F32)
                dk2 = jnp.zeros((t, LANES), F32)
                dv2 = jnp.zeros((t, LANES), F32)
                for e in range(2):
                    h = 2 * p + e
                    msk = left if e == 0 else jnp.logical_not(left)
                    bias = _wide(cc_ref[h]) - cr_ref[h:h + 1, :]
                    qh, doh, pm, dp = _attn_probs(q2, k2, v2, do2, msk, causal, bias, lse_ref[h])
                    dsc = pm * (dp - _wide(dd_ref[h]))
                    dsb = dsc.astype(BF16)
                    dv2 += _dot_tn(pm.astype(BF16), doh)
                    dk2 += _dot_tn(dsb, qh)
                    dq2 += jnp.dot(dsb, jnp.where(msk, k2, zero), preferred_element_type=F32)
                    dcr_sc[h:h + 1, :] += -_colsum(dsc)
                dq_ref[qrows, lanes] += dq2 * ATT_SCALE
                dk_sc[:, lanes] += dk2
                dv_sc[:, lanes] += dv2

        pl.when(i > j)(functools.partial(tile, False))
        pl.when(i == j)(functools.partial(tile, True))

        @pl.when(i == nblk - 1)
        def _():
            dk_ref[...] = dk_sc[...]
            dv_ref[...] = dv_sc[...]
            dcr_ref[...] = dcr_sc[...]

    qblk = pl.BlockSpec((t, D_HALF), lambda j, i: (jnp.maximum(i, j), 0))
    qcol = pl.BlockSpec((N_HEADS, t, LANES), lambda j, i: (0, jnp.maximum(i, j), 0))
    kblk = pl.BlockSpec((t, D_HALF), lambda j, i: (j, 0))
    return pl.pallas_call(
        body, name="fox_attn_bwd", grid=(nblk, nblk),
        in_specs=[qblk, kblk, kblk, qblk, qcol, qcol, qcol, pl.BlockSpec((N_HEADS, t), lambda j, i: (0, j))],
        out_specs=[pl.BlockSpec((s, D_HALF), lambda j, i: (0, 0)), kblk, kblk,
                   pl.BlockSpec((N_HEADS, t), lambda j, i: (0, j))],
        out_shape=[jax.ShapeDtypeStruct((s, D_HALF), F32)] * 3 + [jax.ShapeDtypeStruct((N_HEADS, s), F32)],
        scratch_shapes=[pltpu.VMEM((t, D_HALF), F32), pltpu.VMEM((t, D_HALF), F32), pltpu.VMEM((N_HEADS, t), F32)],
        compiler_params=_params("arbitrary", "arbitrary"),
    )(q, k, v, do, lse, dd, cc, cr)


def _fox_prep_bwd(u_b, h_t, dq, dk, dv, dgate, dcum, fb, qg, kg, tm=256):
    s = u_b.shape[0]
    nb = s // tm
    d = h_t.shape[0]

    def body(ub_ref, ht_ref, dq_ref, dk_ref, dv_ref, dg_ref, dc_ref, fb_ref, qg_ref, kg_ref,
             du_ref, dwb_ref, dqg_ref, dkg_ref, dfb_ref, carry):
        i = pl.program_id(0)

        @pl.when(i == 0)
        def _():
            carry[...] = jnp.zeros_like(carry)
            dwb_ref[...] = jnp.zeros_like(dwb_ref)
            dqg_ref[...] = jnp.zeros_like(dqg_ref)
            dkg_ref[...] = jnp.zeros_like(dkg_ref)
            dfb_ref[...] = jnp.zeros_like(dfb_ref)

        bd = _head_ones()
        for lo, g_ref, d_ref, dgain_ref in ((0, qg_ref, dq_ref, dqg_ref), (512, kg_ref, dk_ref, dkg_ref)):
            gain = g_ref[...]
            xh, rinv, _ = _head_rms(ub_ref[:, lo:lo + 512], gain, bd)
            dn = d_ref[...]
            dgain_ref[...] += _colsum(dn * xh)
            dxh = dn * gain
            du_ref[:, lo:lo + 512] = rinv * (dxh - xh * (_head_sum(dxh * xh, bd) * (1.0 / HEAD)))
        du_ref[:, 1024:1536] = dv_ref[...]
        du_ref[:, 1536:2048] = dg_ref[...]
        lane = lax.broadcasted_iota(jnp.int32, (1, LANES), 1)
        dc = dc_ref[...]
        dlogf = _exact_dot(dc, _tri(tm, False), ones_first=True) + carry[...]
        carry[...] += _colsum(dc)
        fl = ub_ref[:, 2048:2176] + fb_ref[...]
        dfl = jnp.where(lane < N_HEADS, dlogf * (1.0 - _sigmoid(fl)), 0.0)
        du_ref[:, 2048:2176] = dfl
        dfb_ref[...] += _colsum(dfl)
        dwb_ref[...] += jnp.dot(ht_ref[...], du_ref[...].astype(BF16), preferred_element_type=F32)

    rev = lambda w: pl.BlockSpec((tm, w), lambda i: (nb - 1 - i, 0))
    vec = lambda w: pl.BlockSpec((1, w), lambda i: (0, 0))
    return pl.pallas_call(
        body, name="fox_prep_bwd", grid=(nb,),
        in_specs=[rev(SEC), pl.BlockSpec((d, tm), lambda i: (0, nb - 1 - i))] + [rev(D_HALF)] * 4
                 + [rev(LANES), vec(LANES), vec(D_HALF), vec(D_HALF)],
        out_specs=[rev(SEC), pl.BlockSpec((d, SEC), lambda i: (0, 0)), vec(D_HALF), vec(D_HALF), vec(LANES)],
        out_shape=[jax.ShapeDtypeStruct((s, SEC), F32), jax.ShapeDtypeStruct((d, SEC), F32),
                   jax.ShapeDtypeStruct((1, D_HALF), F32), jax.ShapeDtypeStruct((1, D_HALF), F32),
                   jax.ShapeDtypeStruct((1, LANES), F32)],
        scratch_shapes=[pltpu.VMEM((1, LANES), F32)],
        compiler_params=_params("arbitrary"),
    )(u_b, h_t, dq, dk, dv, dgate, dcum, fb, qg, kg)


def _merge(y, r, k, v, gate_a, o, u_b, h, x, tgt, w_g, wa, wb, wo, fg, lw, lb, rk, tm=256):
    s, d = x.shape

    def body(y_ref, r_ref, k_ref, v_ref, ga_ref, o_ref, gb_ref, h_ref, x_ref, t_ref, wg_ref, wa_ref, wb_ref, wo_ref,
             fg_ref, lw_ref, lb_ref, rk_ref,
             dx2_ref, dy_ref, drb_ref, dkb_ref, dvb_ref, dga_ref, do_ref, dgb_ref, dug_ref,
             dwa_ref, dwb_ref, dwo_ref, dfg_ref, loss_ref, dlw_ref, dlb_ref, drk_ref):
        i = pl.program_id(0)

        @pl.when(i == 0)
        def _():
            for ref in (dwa_ref, dwb_ref, dwo_ref, dfg_ref, loss_ref, dlw_ref, dlb_ref, drk_ref):
                ref[...] = jnp.zeros_like(ref)

        bd = _head_ones()
        wa_v, wb_v, wo_v, fg_v = wa_ref[...], wb_ref[...], wo_ref[...], fg_ref[...]
        rv, kv, vv, ga, lw_v, rk_v = r_ref[...], k_ref[...], v_ref[...], ga_ref[...], lw_ref[...], rk_ref[...]
        yn, rstd, rkk, sga, pre = _rwkv_post_math(y_ref[...], rv, kv, vv, ga, lw_v, lb_ref[...], rk_v, bd)
        silu_a = ga * sga
        gb, ov = gb_ref[...], o_ref[...]
        sgb = _sigmoid(gb)
        silu_b = gb * sgb
        ma = (pre * silu_a).astype(BF16)
        mb = (ov * silu_b).astype(BF16)
        ya = jnp.dot(ma, wa_v, preferred_element_type=F32)
        yb = jnp.dot(mb, wb_v, preferred_element_type=F32)
        ug = jnp.dot(h_ref[...], wg_ref[...], preferred_element_type=F32)
        sa = _sigmoid(ug[:, 0:d])
        sb = _sigmoid(ug[:, d:2 * d])
        merged = (sa * ya + sb * yb).astype(BF16)
        x2 = x_ref[...] + jnp.dot(merged, wo_v, preferred_element_type=F32)
        r2 = lax.rsqrt(jnp.mean(x2 * x2, axis=-1, keepdims=True) + RMS_EPS)
        x2h = x2 * r2
        err = x2h * fg_v - t_ref[...]
        loss_ref[...] += _colsum(err * err)
        dyo = err * (1.0 / d)
        dfg_ref[...] += _colsum(dyo * x2h)
        dx2h = dyo * fg_v
        dx2 = r2 * (dx2h - x2h * jnp.mean(dx2h * x2h, axis=-1, keepdims=True))
        dx2_ref[...] = dx2
        dx2b = dx2.astype(BF16)
        dmerged = _dot_nt(dx2b, wo_v)
        dwo_ref[...] += _dot_tn(merged, dx2b)
        dya = dmerged * sa
        dyb = dmerged * sb
        dug_ref[:, 0:d] = dya * ya * (1.0 - sa)
        dug_ref[:, d:2 * d] = dyb * yb * (1.0 - sb)
        dyab = dya.astype(BF16)
        dybb = dyb.astype(BF16)
        dwa_ref[...] += _dot_tn(ma, dyab)
        dwb_ref[...] += _dot_tn(mb, dybb)
        dmb = _dot_nt(dybb, wb_v)
        do_ref[...] = (dmb * silu_b).astype(BF16)
        dgb_ref[...] = dmb * ov * (sgb * (1.0 + gb * (1.0 - sgb)))
        dma = _dot_nt(dyab, wa_v)
        dga_ref[...] = dma * pre * (sga * (1.0 + ga * (1.0 - sga)))
        dpre = dma * silu_a
        dlw_ref[...] += _colsum(dpre * yn)
        dlb_ref[...] += _colsum(dpre)
        dyn = dpre * lw_v
        m1 = _head_sum(dyn, bd) * (1.0 / HEAD)
        m2 = _head_sum(dyn * yn, bd) * (1.0 / HEAD)
        dy_ref[...] = rstd * (dyn - m1 - yn * m2)
        dvb_ref[...] = dpre * rkk
        drkk = _head_sum(dpre * vv, bd)
        drb_ref[...] = drkk * kv * rk_v
        dkb_ref[...] = drkk * rv * rk_v
        drk_ref[...] += _colsum(drkk * rv * kv)

    row = lambda w: pl.BlockSpec((tm, w), lambda i: (i, 0))
    full = lambda a: pl.BlockSpec(a.shape, lambda i: (0, 0))
    once = lambda a: pl.BlockSpec(a.shape, lambda i: (0, 0), pipeline_mode=pl.Buffered(1))
    half = jax.ShapeDtypeStruct((s, D_HALF), F32)
    fshape = lambda a: jax.ShapeDtypeStruct(a.shape, F32)
    return pl.pallas_call(
        body, name="merge_fwd_bwd", grid=(s // tm,),
        in_specs=[row(D_HALF)] * 6 + [pl.BlockSpec((tm, D_HALF), lambda i: (i, 3)), row(d), row(d), row(d),
                                      once(w_g), once(wa), once(wb), once(wo), full(fg), full(lw), full(lb), full(rk)],
        out_specs=[row(d)] + [row(D_HALF)] * 7 + [row(GATE_COLS), full(wa), full(wb), full(wo), full(fg), full(fg),
                                                   full(lw), full(lb), full(rk)],
        out_shape=[jax.ShapeDtypeStruct((s, d), F32)] + [half] * 5 + [jax.ShapeDtypeStruct((s, D_HALF), BF16), half,
                                                                    jax.ShapeDtypeStruct((s, GATE_COLS), F32),
                                                                    fshape(wa), fshape(wb), fshape(wo), fshape(fg),
                                                                    fshape(fg), fshape(lw), fshape(lb), fshape(rk)],
        compiler_params=_params("arbitrary"),
    )(y, r, k, v, gate_a, o, u_b, h, x, tgt, w_g, wa, wb, wo, fg, lw, lb, rk)


def _lora_weight(w_up, a_up):
    z = jnp.zeros((LORA, D_HALF), w_up.dtype)
    return jnp.concatenate([jnp.concatenate([w_up, z], axis=1), jnp.concatenate([z, a_up], axis=1)], axis=0)


def _device_grads(x, tgt, p, w_a, w_up, a_up, late_weights, fwd_exchange=None, bwd_exchange=None, tail_exchange=None):
    wl = _lora_weight(w_up, a_up)
    rk = p["r_k"].reshape(1, D_HALF)
    fb = jnp.pad(p["f_bias"], ((0, 0), (0, LANES - N_HEADS)))
    qg = jnp.tile(p["q_norm_g"], (1, N_HEADS))
    kg = jnp.tile(p["k_norm_g"], (1, N_HEADS))
    fg = p["final_norm_g"].reshape(1, D_MODEL)
    mixer = (p["shift_mu"], wl, p["w0"], p["a0"], p["k_k"], p["k_a"])

    h, u_a, r, dec, k, v, av, bv, gate_a = _rwkv_front(x, p["norm_g"], w_a, *mixer)
    y, st, arrived = _wkv_fwd(r, dec, k, av, bv, v, fwd_exchange)

    w_b, w_g, w_out_a, w_out_b, w_out = late_weights(arrived)
    u_b, q, kn, vb, cc, cr = _fox_front(h, w_b, fb, qg, kg)
    o, lse = _attn_fwd(q, kn, vb, cc, cr)

    (dx2, dy, dr_b, dk_b, dv_b, dgate_a, do, dgate_b, du_g, dwa, dwb, dwo, dfg, loss_vec, dlw, dlb, drk) = _merge(
        y, r, k, v, gate_a, o, u_b, h, x, tgt, w_g, w_out_a, w_out_b, w_out, fg, p["lnx_w"], p["lnx_b"], rk)

    dd = _attn_bwd_rowdot(q, kn, vb, do, lse, cc, cr)
    dq, dk_att, dv_att, dcr = _attn_bwd(q, kn, vb, do, lse, dd, cc, cr)
    dcum = jnp.pad(dcr.T, ((0, 0), (0, LANES - N_HEADS)))
    h_t = h.T
    du_b, dw_b, dqg, dkg, dfb = _fox_prep_bwd(u_b, h_t, dq, dk_att, dv_att, dgate_b, dcum, fb, qg, kg)
    dw_g = _matmul_tn_acc(h_t, du_g, "dw_gate")

    scan_grads, sent = _wkv_bwd(r, dec, k, av, bv, v, dy, st,
                                bwd_exchange(dw_b, dw_g, dwa, dwb, dwo) if bwd_exchange else None)
    du_a, dw_a, dmu, dwl, dw0, da0, dkkw, dkaw = _rwkv_prep_bwd(
        u_a, h_t, (*scan_grads, dr_b, dk_b, dv_b, dgate_a), *mixer)
    dw_up, da_up = dwl[:LORA, :D_HALF], dwl[LORA:, D_HALF:]
    sent_last = _run_on_sequencer(tail_exchange(dw_a, dw_up, da_up), "scatter_tail", 1) if tail_exchange else []
    grad_x, dnorm_g, _ = _inproj_bwd(du_a, du_b, du_g, w_a, w_b, w_g, x, dx2, p["norm_g"])

    grads = dict(
        norm_g=dnorm_g, w_in=(dw_a, dw_b, dw_g), shift_mu=dmu,
        w_lora_up=dw_up, w0=dw0, a_lora_up=da_up, a0=da0, k_k=dkkw, k_a=dkaw,
        r_k=drk.reshape(1, N_HEADS, HEAD), lnx_w=dlw, lnx_b=dlb, f_bias=dfb[:, :N_HEADS],
        q_norm_g=dqg.reshape(N_HEADS, HEAD).sum(axis=0, keepdims=True),
        k_norm_g=dkg.reshape(N_HEADS, HEAD).sum(axis=0, keepdims=True),
        w_out_a=dwa, w_out_b=dwb, w_out=dwo, final_norm_g=dfg.reshape(D_MODEL))
    return loss_vec, grad_x, grads, sent, sent_last


CHIP_FLIPS = ((1, 0), (0, 1), (1, 1))
ANY = pl.BlockSpec(memory_space=pl.ANY)


def _position():
    return lax.axis_index("x"), lax.axis_index("y"), lax.axis_index("c")


def _flip(v, f):
    return 1 - v if f else v


def _both(a, b):
    if a is None:
        return b
    return a if b is None else jnp.logical_and(a, b)


def _when(cond, fn):
    if cond is None:
        fn()
    else:
        pl.when(cond)(fn)


class _Moves:
    def __init__(self, send_sems, recv_sems, local_sems):
        self.send_sems, self.recv_sems, self.local_sems = send_sems, recv_sems, local_sems
        self.remote, self.local = [], []

    def send(self, src, dst, peer, landing, send_if=None, recv_if=None):
        k = len(self.remote)
        sems = dict(send_sem=self.send_sems.at[k], recv_sem=self.recv_sems.at[k], device_id=peer, device_id_type=MESH)
        out = pltpu.make_async_remote_copy(src_ref=src, dst_ref=dst, **sems)
        arrival = pltpu.make_async_remote_copy(src_ref=src, dst_ref=landing, **sems)
        self.remote.append((out, arrival, send_if, recv_if))

    def copy(self, src, dst, cond=None):
        cp = pltpu.make_async_copy(src, dst, self.local_sems.at[len(self.local)])
        self.local.append((cp, cond))

    def start(self, also=None):
        for cp, cond in self.local:
            _when(_both(also, cond), cp.start)
        for out, _, send_if, _ in self.remote:
            _when(_both(also, send_if), out.start)

    def wait_arrivals(self, also=None):
        for _, arrival, _, recv_if in self.remote:
            _when(_both(also, recv_if), arrival.wait_recv)

    def wait_sent(self, also=None):
        for out, _, send_if, _ in self.remote:
            _when(_both(also, send_if), out.wait_send)
        for cp, cond in self.local:
            _when(_both(also, cond), cp.wait)

    def wait(self, also=None):
        self.wait_arrivals(also)
        self.wait_sent(also)


class _Exchange:
    def __init__(self, operands, out_shapes, n_remote, n_local, build, n_relay=0, relay=None):
        self.operands, self.out_shapes = list(operands), list(out_shapes)
        self.n_remote, self.n_local, self.build = n_remote, n_local, build
        self.n_relay, self.relay = n_relay, relay

    def scratch(self):
        return [pltpu.SemaphoreType.DMA((self.n_remote,)), pltpu.SemaphoreType.DMA((self.n_remote,)),
                pltpu.SemaphoreType.DMA((max(self.n_local, 1),))]

    def moves(self, in_refs, out_refs, sems):
        mv = _Moves(*sems)
        self.build(mv, in_refs, out_refs)
        return mv

    def run_alone(self, name):
        n_in, n_out = len(self.operands), len(self.out_shapes)
        relay_scratch = [pltpu.SemaphoreType.DMA((self.n_relay,))] * 2 if self.relay else []

        def body(*refs):
            ins, outs, sems = refs[:n_in], refs[n_in:n_in + n_out], refs[n_in + n_out:]
            mv = self.moves(ins, outs, sems[:3])
            mv.start()
            mv.wait_arrivals()
            if self.relay:
                passed = _Moves(sems[3], sems[4], None)
                self.relay(passed, ins, outs)
                passed.start()
                passed.wait()
            mv.wait_sent()

        return pl.pallas_call(
            body, name=name, in_specs=[ANY] * n_in, out_specs=[ANY] * n_out, out_shape=self.out_shapes,
            scratch_shapes=self.scratch() + relay_scratch, compiler_params=pltpu.CompilerParams(has_side_effects=True),
        )(*self.operands)


def _run_on_sequencer(exchange, name, collective_id):
    ins = [jax.new_ref(a, memory_space=pltpu.MemorySpace.HBM) for a in exchange.operands]
    outs = [jax.empty_ref(s, memory_space=pltpu.MemorySpace.HBM) for s in exchange.out_shapes]
    relay_scratch = [pltpu.SemaphoreType.DMA((exchange.n_relay,))] * 2 if exchange.relay else []

    def launch(*sems):
        x, y, c = _position()
        peers = [(_flip(x, fx), _flip(y, fy), c) for fx, fy in CHIP_FLIPS] + ([(x, y, 1 - c)] if exchange.relay else [])
        barrier = pltpu.get_barrier_semaphore()
        for peer in peers:
            pl.semaphore_signal(barrier, inc=1, device_id=peer, device_id_type=MESH)
        pl.semaphore_wait(barrier, len(peers))
        moves = exchange.moves(ins, outs, sems[:3])
        moves.start()
        moves.wait_arrivals()
        if exchange.relay:
            passed = _Moves(sems[3], sems[4], None)
            exchange.relay(passed, ins, outs)
            passed.start()
            passed.wait()
        moves.wait_sent()

    pl.kernel(launch, mesh=plsc.ScalarSubcoreMesh(axis_name="sequencer", num_cores=1), name=name,
              scratch_types=tuple(exchange.scratch() + relay_scratch),
              compiler_params=pltpu.CompilerParams(collective_id=collective_id))()
    return [o[...] for o in outs]


def _row_major_copy(a, name):
    r, c = a.shape
    tr = _row_tile(r)

    def body(a_ref, o_ref):
        o_ref[...] = a_ref[...]

    blk = pl.BlockSpec((tr, c), lambda i: (i, 0))
    return pl.pallas_call(body, name=name, grid=(r // tr,), in_specs=[blk], out_specs=blk,
                          out_shape=jax.ShapeDtypeStruct(a.shape, a.dtype), compiler_params=_params("parallel"))(a)


def _is_chip(x, y, chip):
    return jnp.logical_and(x == chip // 2, y == chip % 2)


def _gather_exchange(from_chip, from_all, split=()):
    n1, n2 = len(from_chip), len(from_all)

    def rows_of(t, c):
        half = from_chip[t][1].shape[0] // 2
        return pl.ds(c * half, half)

    def build(mv, ins, outs):
        x, y, c = _position()
        me = 2 * x + y
        for t, (chip, _) in enumerate(from_chip):
            mv.copy(ins[t], outs[t], cond=_is_chip(x, y, chip))
        for t in range(n2):
            mv.copy(ins[n1 + t], outs[n1 + t].at[me])
        for fx, fy in CHIP_FLIPS:
            px, py = _flip(x, fx), _flip(y, fy)
            peer = (px, py, c)
            for t, (chip, _) in enumerate(from_chip):
                part = rows_of(t, c) if t in split else slice(None)
                mv.send(ins[t].at[part], outs[t].at[part], peer, landing=outs[t].at[part],
                        send_if=_is_chip(x, y, chip), recv_if=_is_chip(px, py, chip))
            for t in range(n2):
                mv.send(ins[n1 + t], outs[n1 + t].at[me], peer, landing=outs[n1 + t].at[2 * px + py])

    def relay(mv, ins, outs):
        x, y, c = _position()
        for t in split:
            came = jnp.logical_not(_is_chip(x, y, from_chip[t][0]))
            mv.send(outs[t].at[rows_of(t, c)], outs[t].at[rows_of(t, c)], (x, y, 1 - c),
                    landing=outs[t].at[rows_of(t, 1 - c)], send_if=came, recv_if=came)

    arrays = [a for _, a in from_chip] + list(from_all)
    shapes = [jax.ShapeDtypeStruct(a.shape, a.dtype) for _, a in from_chip]
    shapes += [jax.ShapeDtypeStruct((N_CHIPS,) + a.shape, a.dtype) for a in from_all]
    return _Exchange(arrays, shapes, len(CHIP_FLIPS) * (n1 + n2), n1 + n2, build,
                     n_relay=len(split), relay=relay if split else None)


def _scatter_exchange(to_chip, to_all):
    n1, n2 = len(to_chip), len(to_all)

    def build(mv, ins, outs):
        x, y, c = _position()
        for f, (fx, fy) in enumerate(CHIP_FLIPS):
            px, py = _flip(x, fx), _flip(y, fy)
            peer = (px, py, c)
            for t, (chip, _) in enumerate(to_chip):
                mv.send(ins[t], outs[t].at[f], peer, landing=outs[t].at[f],
                        send_if=_is_chip(px, py, chip), recv_if=_is_chip(x, y, chip))
            for t in range(n2):
                mv.send(ins[n1 + t].at[2 * px + py], outs[n1 + t].at[f], peer, landing=outs[n1 + t].at[f])

    arrays = [a for _, a in to_chip] + list(to_all)
    shapes = [jax.ShapeDtypeStruct((len(CHIP_FLIPS),) + a.shape, a.dtype) for _, a in to_chip]
    shapes += [jax.ShapeDtypeStruct((len(CHIP_FLIPS),) + a.shape[1:], a.dtype) for a in to_all]
    return _Exchange(arrays, shapes, len(CHIP_FLIPS) * (n1 + n2), 0, build)


def _swap_sibling(tensors, name):
    n = len(tensors)

    def body(*refs):
        ins, outs = refs[:n], refs[n:2 * n]
        send_sems, recv_sems = refs[2 * n:]
        x, y, c = _position()
        copies = [pltpu.make_async_remote_copy(
            src_ref=ins[t], dst_ref=outs[t], send_sem=send_sems.at[t], recv_sem=recv_sems.at[t],
            device_id=(x, y, 1 - c), device_id_type=MESH) for t in range(n)]
        for cp in copies:
            cp.start()
        for cp in copies:
            cp.wait_recv()
        for cp in copies:
            cp.wait_send()

    return pl.pallas_call(
        body, name=name, in_specs=[ANY] * n, out_specs=[ANY] * n,
        out_shape=[jax.ShapeDtypeStruct(a.shape, a.dtype) for a in tensors],
        scratch_shapes=[pltpu.SemaphoreType.DMA((n,)), pltpu.SemaphoreType.DMA((n,))],
        compiler_params=pltpu.CompilerParams(has_side_effects=True),
    )(*tensors)


def _allreduce_small(slab):
    stages = 3

    def body(x_ref, o_ref, buf, send_sems, recv_sems):
        x, y, c = _position()
        peers = ((1 - x, y, c), (x, 1 - y, c), (x, y, 1 - c))
        o_ref[...] = x_ref[...]
        for k, peer in enumerate(peers):
            cp = pltpu.make_async_remote_copy(src_ref=o_ref, dst_ref=buf.at[k], send_sem=send_sems.at[k],
                                              recv_sem=recv_sems.at[k], device_id=peer, device_id_type=MESH)
            cp.start()
            cp.wait()
            o_ref[...] = o_ref[...] + buf[k]

    return pl.pallas_call(
        body, name="allreduce_small",
        in_specs=[pl.BlockSpec(memory_space=pltpu.VMEM)], out_specs=pl.BlockSpec(memory_space=pltpu.VMEM),
        out_shape=jax.ShapeDtypeStruct(slab.shape, slab.dtype),
        scratch_shapes=[pltpu.VMEM((stages,) + slab.shape, slab.dtype),
                        pltpu.SemaphoreType.DMA((stages,)), pltpu.SemaphoreType.DMA((stages,))],
        compiler_params=pltpu.CompilerParams(has_side_effects=True),
    )(slab)


def _row_tile(r):
    return min(r, 256)


def _sum4(stack, recv, me):
    _, r, c = stack.shape
    tr = _row_tile(r)

    def body(me_ref, own_ref, recv_ref, o_ref):
        o_ref[...] = (((own_ref[...] + recv_ref[0].astype(F32)) + recv_ref[1].astype(F32))
                      + recv_ref[2].astype(F32))

    return pl.pallas_call(
        body, name="sum_partials",
        grid_spec=pltpu.PrefetchScalarGridSpec(
            num_scalar_prefetch=1, grid=(r // tr,),
            in_specs=[pl.BlockSpec((None, tr, c), lambda i, me_ref: (me_ref[0], i, 0)),
                      pl.BlockSpec((len(CHIP_FLIPS), tr, c), lambda i, me_ref: (0, i, 0))],
            out_specs=pl.BlockSpec((tr, c), lambda i, me_ref: (i, 0))),
        out_shape=jax.ShapeDtypeStruct((r, c), F32), compiler_params=_params("parallel"),
    )(me, stack, recv)


def _sum_block(own, recv):
    r, c = own.shape
    tr = _row_tile(r)

    def body(own_ref, recv_ref, o_ref):
        o_ref[...] = (((own_ref[...] + recv_ref[0].astype(F32)) + recv_ref[1].astype(F32))
                      + recv_ref[2].astype(F32))

    return pl.pallas_call(
        body, name="sum_block", grid=(r // tr,),
        in_specs=[pl.BlockSpec((tr, c), lambda i: (i, 0)), pl.BlockSpec((len(CHIP_FLIPS), tr, c), lambda i: (0, i, 0))],
        out_specs=pl.BlockSpec((tr, c), lambda i: (i, 0)),
        out_shape=jax.ShapeDtypeStruct((r, c), F32), compiler_params=_params("parallel"),
    )(own, recv)


def _adamw_math(w, g, m, v):
    m = ADAM_B1 * m + (1.0 - ADAM_B1) * g
    v = ADAM_B2 * v + (1.0 - ADAM_B2) * (g * g)
    m_hat = m / (1.0 - ADAM_B1 ** ADAM_STEP)
    v_hat = v / (1.0 - ADAM_B2 ** ADAM_STEP)
    delta = -ADAM_LR * (m_hat / (jnp.sqrt(v_hat) + ADAM_EPS) + ADAM_WD * w)
    return delta, m, v


def _adamw(w, m, v, g_parts, name):
    r, c = w.shape
    tr = _row_tile(r)
    n = len(g_parts)

    def body(*refs):
        w_ref, m_ref, v_ref = refs[:3]
        g_refs = refs[3:3 + n]
        g_out, d_out, m_out, v_out = refs[3 + n:]
        g = g_refs[0][...]
        for ref in g_refs[1:]:
            g = g + ref[...]
        g_out[...] = g
        d_out[...], m_out[...], v_out[...] = _adamw_math(w_ref[...], g, m_ref[...], v_ref[...])

    blk = pl.BlockSpec((tr, c), lambda i: (i, 0))
    return pl.pallas_call(
        body, name=name, grid=(r // tr,), in_specs=[blk] * (3 + n), out_specs=[blk] * 4,
        out_shape=[jax.ShapeDtypeStruct((r, c), F32)] * 4, compiler_params=_params("parallel"),
    )(w, m, v, *g_parts)


def _adamw_small(total, w, m, v):
    sizes = [w[n].size for n in SMALL]
    flat = lambda d: [d[n].reshape(1, -1) for n in SMALL]
    k = len(SMALL)

    def body(*refs):
        total_ref, w_refs, m_refs, v_refs = refs[0], refs[1:1 + k], refs[1 + k:1 + 2 * k], refs[1 + 2 * k:1 + 3 * k]
        outs = refs[1 + 3 * k:]
        for i, size in enumerate(sizes):
            g = total_ref[i:i + 1, 0:size]
            outs[i][...] = g
            outs[k + i][...], outs[2 * k + i][...], outs[3 * k + i][...] = _adamw_math(
                w_refs[i][...], g, m_refs[i][...], v_refs[i][...])

    res = pl.pallas_call(
        body, name="adamw_small", out_shape=[jax.ShapeDtypeStruct((1, size), F32) for size in sizes] * 4,
        compiler_params=_params(),
    )(total, *flat(w), *flat(m), *flat(v))
    return [{n: res[j * k + i].reshape(w[n].shape) for i, n in enumerate(SMALL)} for j in range(4)]


SHARDED = ("w_in", "w_lora_up", "a_lora_up", "w_out_a", "w_out_b", "w_out")
ROW_SHARDED = ("w_out",)
SMALL = ("norm_g", "shift_mu", "w0", "a0", "k_k", "k_a", "r_k", "lnx_w", "lnx_b", "f_bias", "q_norm_g", "k_norm_g",
         "final_norm_g")
WEIGHTS = ("norm_g", "w_in", "shift_mu", "w_lora_up", "w0", "a_lora_up", "a0", "k_k", "k_a", "r_k", "lnx_w", "lnx_b",
           "f_bias", "q_norm_g", "k_norm_g", "w_out_a", "w_out_b", "w_out", "final_norm_g")
SLAB_ROWS = 16
SLAB_COLS = SEC


def _to_slab(named, extra=None):
    rows = [jnp.pad(named[n].reshape(1, -1), ((0, 0), (0, SLAB_COLS - named[n].size))) for n in SMALL]
    if extra is not None:
        rows.append(jnp.pad(extra.reshape(1, -1), ((0, 0), (0, SLAB_COLS - extra.size))))
    rows.append(jnp.zeros((SLAB_ROWS - len(rows), SLAB_COLS), F32))
    return jnp.concatenate(rows, axis=0)


def _by_chip(g, name):
    if name in ROW_SHARDED:
        return g.reshape(N_CHIPS, g.shape[0] // N_CHIPS, g.shape[1])
    r, c = g.shape
    return g.reshape(r, N_CHIPS, c // N_CHIPS).transpose(1, 0, 2)


def _from_chips(stack, name):
    if name in ROW_SHARDED:
        return stack.reshape(-1, stack.shape[2])
    _, r, c = stack.shape
    return stack.transpose(1, 0, 2).reshape(r, N_CHIPS * c)


def kernel(x, norm_g, w_in, shift_mu, w_lora_up, w0, a_lora_up, a0, k_k, k_a, r_k, lnx_w, lnx_b, f_bias, q_norm_g, k_norm_g, w_out_a, w_out_b, w_out, final_norm_g, loss_target, m_norm_g, m_w_in, m_shift_mu, m_w_lora_up, m_w0, m_a_lora_up, m_a0, m_k_k, m_k_a, m_r_k, m_lnx_w, m_lnx_b, m_f_bias, m_q_norm_g, m_k_norm_g, m_w_out_a, m_w_out_b, m_w_out, m_final_norm_g, v_norm_g, v_w_in, v_shift_mu, v_w_lora_up, v_w0, v_a_lora_up, v_a0, v_k_k, v_k_a, v_r_k, v_lnx_w, v_lnx_b, v_f_bias, v_q_norm_g, v_k_norm_g, v_w_out_a, v_w_out_b, v_w_out, v_final_norm_g):
    w = dict(norm_g=norm_g, w_in=w_in, shift_mu=shift_mu, w_lora_up=w_lora_up, w0=w0, a_lora_up=a_lora_up, a0=a0,
             k_k=k_k, k_a=k_a, r_k=r_k, lnx_w=lnx_w, lnx_b=lnx_b, f_bias=f_bias, q_norm_g=q_norm_g,
             k_norm_g=k_norm_g, w_out_a=w_out_a, w_out_b=w_out_b, w_out=w_out, final_norm_g=final_norm_g)
    m = dict(norm_g=m_norm_g, w_in=m_w_in, shift_mu=m_shift_mu, w_lora_up=m_w_lora_up, w0=m_w0,
             a_lora_up=m_a_lora_up, a0=m_a0, k_k=m_k_k, k_a=m_k_a, r_k=m_r_k, lnx_w=m_lnx_w, lnx_b=m_lnx_b,
             f_bias=m_f_bias, q_norm_g=m_q_norm_g, k_norm_g=m_k_norm_g, w_out_a=m_w_out_a, w_out_b=m_w_out_b,
             w_out=m_w_out, final_norm_g=m_final_norm_g)
    v = dict(norm_g=v_norm_g, w_in=v_w_in, shift_mu=v_shift_mu, w_lora_up=v_w_lora_up, w0=v_w0,
             a_lora_up=v_a_lora_up, a0=v_a0, k_k=v_k_k, k_a=v_k_a, r_k=v_r_k, lnx_w=v_lnx_w, lnx_b=v_lnx_b,
             f_bias=v_f_bias, q_norm_g=v_q_norm_g, k_norm_g=v_k_norm_g, w_out_a=v_w_out_a, w_out_b=v_w_out_b,
             w_out=v_w_out, final_norm_g=v_final_norm_g)
    shapes = {n: w[n].shape for n in WEIGHTS}

    shard = {n: w[n][0].astype(BF16) for n in SHARDED}
    late = ("w_out_a", "w_out_b", "w_out")
    loras = ("w_lora_up", "a_lora_up")
    w_in_head, w_in_tail = shard["w_in"][:, :A_TAIL], shard["w_in"][:, A_TAIL:]
    shard0, shard1_head, up_stack, aup_stack = _run_on_sequencer(_gather_exchange(
        [(0, shard["w_in"]), (1, w_in_head)], [shard[n] for n in loras], split=(0,)), "gather_early", 2)
    moments = (_row_major_copy(m["w_in"][0], "m_w_in_rows"), _row_major_copy(v["w_in"][0], "v_w_in_rows"))
    shard0, moments = lax.optimization_barrier((shard0, moments))
    w_a = jnp.concatenate([shard0, shard1_head], axis=1)

    def late_weights(arrived):
        shard1_tail, shard2, shard3 = arrived[:3]
        w_b = jnp.concatenate([shard1_tail, shard2[:, :B_TAIL], jnp.zeros((D_MODEL, SEC - FOX_REAL), BF16)], axis=1)
        w_g = jnp.concatenate([shard2[:, B_TAIL:], shard3], axis=1)
        return (w_b, w_g, *[_from_chips(s, n) for n, s in zip(late, arrived[3:])])

    own = {}

    def bwd_exchange(dw_b, dw_g, dwa, dwb, dwo):
        own["tail1"] = dw_b[:, :B_HEAD]
        own["block2"] = jnp.concatenate([dw_b[:, B_HEAD:FOX_REAL], dw_g[:, :G_HEAD]], axis=1)
        own["block3"] = dw_g[:, G_HEAD:]
        own.update({n: _by_chip(g, n) for n, g in zip(late, (dwa, dwb, dwo))})
        return _scatter_exchange([(1, own["tail1"].astype(BF16)), (2, own["block2"].astype(BF16)),
                                  (3, own["block3"].astype(BF16))], [own[n].astype(BF16) for n in late])

    def tail_exchange(dw_a, dw_up, da_up):
        own["block0"], own["head1"] = dw_a[:, :SHARD_COLS], dw_a[:, SHARD_COLS:]
        own.update({n: _by_chip(g, n) for n, g in zip(loras, (dw_up, da_up))})
        return _scatter_exchange([(0, own["block0"].astype(BF16)), (1, own["head1"].astype(BF16))],
                                 [own[n].astype(BF16) for n in loras])

    small = {n: w[n] for n in SMALL}
    loss_vec, grad_x, grads, sent, sent_last = _device_grads(
        x[0], loss_target[0], small, w_a, _from_chips(up_stack, "w_lora_up"), _from_chips(aup_stack, "a_lora_up"),
        late_weights, _gather_exchange([(1, w_in_tail), (2, shard["w_in"]), (3, shard["w_in"])], [shard[n] for n in late]),
        bwd_exchange, tail_exchange)

    total = _allreduce_small(_to_slab(grads, extra=loss_vec))
    loss = (0.5 / D_MODEL) * jnp.sum(total[len(SMALL)])
    out_g, out_d, out_m, out_v = _adamw_small(total, w, m, v)

    xpos, ypos, _ = _position()
    me = (2 * xpos + ypos).astype(jnp.int32).reshape(1)
    core_sum = {n: _sum4(own[n], r, me) for n, r in zip(late, sent[3:])}
    theirs = dict(zip(late, _swap_sibling([core_sum[n] for n in late], "swap_sibling_early")))
    sent_last, out_d["norm_g"], theirs = lax.optimization_barrier((sent_last, out_d["norm_g"], theirs))
    core_sum["w_in"] = lax.switch(me[0], [
        lambda: _sum_block(own["block0"], sent_last[0]),
        lambda: jnp.concatenate([_sum_block(own["head1"], sent_last[1]), _sum_block(own["tail1"], sent[0])], axis=1),
        lambda: _sum_block(own["block2"], sent[1]),
        lambda: _sum_block(own["block3"], sent[2])])
    core_sum.update({n: _sum4(own[n], r, me) for n, r in zip(loras, sent_last[2:])})
    rest = ("w_in",) + loras
    theirs.update(zip(rest, _swap_sibling([core_sum[n] for n in rest], "swap_sibling")))
    for n in SHARDED:
        m_n, v_n = moments if n == "w_in" else (m[n][0], v[n][0])
        g, d, m2, v2 = _adamw(w[n][0], m_n, v_n, [core_sum[n], theirs[n]], "adamw_" + n)
        out_g[n], out_d[n], out_m[n], out_v[n] = (a.reshape(shapes[n]) for a in (g, d, m2, v2))

    return (loss, grad_x.reshape(x.shape), *[out_g[n] for n in WEIGHTS], *[out_d[n] for n in WEIGHTS],
            *[out_m[n] for n in WEIGHTS], *[out_v[n] for n in WEIGHTS])
```

```python
import functools
import math

import jax
import jax.numpy as jnp
from jax import lax
from jax.experimental import pallas as pl
from jax.experimental.pallas import tpu as pltpu
from jax.experimental.pallas import tpu_sc as plsc

F32 = jnp.float32
BF16 = jnp.bfloat16

D_MODEL = 1024
D_HALF = 512
HEAD = 64
N_HEADS = 8
LORA = 64
RWKV_COLS = 2176
FOX_REAL = 2056
SEC = 2176
GATE_COLS = 2048
IN_COLS = 6280
N_CHIPS = 4
SHARD_COLS = IN_COLS // N_CHIPS
A_TAIL = RWKV_COLS - SHARD_COLS
B_HEAD = SHARD_COLS - A_TAIL
B_TAIL = FOX_REAL - B_HEAD
G_HEAD = SHARD_COLS - B_TAIL
RMS_EPS = 1e-6
LNX_EPS = 64e-5
ATT_SCALE = HEAD ** -0.5
NEG = -1e30

ADAM_LR = 0.001
ADAM_B1 = 0.9
ADAM_B2 = 0.999
ADAM_EPS = 1e-08
ADAM_WD = 0.01
ADAM_STEP = 10

LANES = 128
SUBLANES = 8
VMEM_LIMIT = 56 * 1024 * 1024
MESH = pl.DeviceIdType.MESH


def _params(*sem):
    return pltpu.CompilerParams(dimension_semantics=sem if sem else None, vmem_limit_bytes=VMEM_LIMIT)


def _sigmoid(x):
    return 1.0 / (1.0 + jnp.exp(-x))


def _log_sigmoid(x):
    return jnp.minimum(x, 0.0) - jnp.log(1.0 + jnp.exp(-jnp.abs(x)))


def _head_ones():
    r = lax.broadcasted_iota(jnp.int32, (LANES, LANES), 0) >> 6
    c = lax.broadcasted_iota(jnp.int32, (LANES, LANES), 1) >> 6
    return (r == c).astype(BF16)


def _split3(x):
    hi = x.astype(BF16)
    r1 = x - hi.astype(F32)
    mid = r1.astype(BF16)
    lo = (r1 - mid.astype(F32)).astype(BF16)
    return hi, mid, lo


def _exact_dot(x, ones_bf16, ones_first=False):
    out = None
    for piece in _split3(x):
        if ones_first:
            t = jnp.dot(ones_bf16, piece, preferred_element_type=F32)
        else:
            t = jnp.dot(piece, ones_bf16, preferred_element_type=F32)
        out = t if out is None else out + t
    return out


def _head_sum(x, bd):
    n = x.shape[1] // LANES
    parts = [_exact_dot(x[:, i * LANES:(i + 1) * LANES], bd) for i in range(n)]
    return parts[0] if n == 1 else jnp.concatenate(parts, axis=1)


def _dot_nt(a, b):
    return lax.dot_general(a, b, (((1,), (1,)), ((), ())), preferred_element_type=F32)


def _dot_tn(a, b):
    return lax.dot_general(a, b, (((0,), (0,)), ((), ())), preferred_element_type=F32)


def _colsum(x):
    return jnp.sum(x, axis=0, keepdims=True)


def _matmul_tn_acc(at, b, name, tk=512):
    m, k = at.shape
    n = b.shape[1]

    def body(a_ref, b_ref, o_ref):
        j = pl.program_id(0)

        @pl.when(j == 0)
        def _():
            o_ref[...] = jnp.zeros_like(o_ref)

        o_ref[...] += jnp.dot(a_ref[...], b_ref[...].astype(BF16), preferred_element_type=F32)

    return pl.pallas_call(
        body, name=name, grid=(k // tk,),
        in_specs=[pl.BlockSpec((m, tk), lambda j: (0, j)), pl.BlockSpec((tk, n), lambda j: (j, 0))],
        out_specs=pl.BlockSpec((m, n), lambda j: (0, 0)),
        out_shape=jax.ShapeDtypeStruct((m, n), F32), compiler_params=_params("arbitrary"),
    )(at, b)


def _inproj_bwd(du_a, du_b, du_g, w_a, w_b, w_g, x, dx2, g, exchange=None, tm=256):
    s, d = x.shape
    nb = s // tm

    def body(*refs):
        ((da_ref, db_ref, dg_ref, wa_ref, wb_ref, wg_ref, x_ref, dx2_ref, g_ref), (gx_ref, gg_ref), _,
         moves) = _split_refs(refs, 9, 2, exchange)
        i = pl.program_id(0)
        if moves:
            moves.start(also=(i == 0))

        @pl.when(i == 0)
        def _():
            gg_ref[...] = jnp.zeros_like(gg_ref)

        dh = _dot_nt(da_ref[...].astype(BF16), wa_ref[...])
        dh += _dot_nt(db_ref[...].astype(BF16), wb_ref[...])
        dh += _dot_nt(dg_ref[...].astype(BF16), wg_ref[...])
        xv = x_ref[...]
        r = lax.rsqrt(jnp.mean(xv * xv, axis=-1, keepdims=True) + RMS_EPS)
        xh = xv * r
        gg_ref[...] += _colsum(dh * xh)
        dxh = dh * g_ref[...]
        gx_ref[...] = dx2_ref[...] + r * (dxh - xh * jnp.mean(dxh * xh, axis=-1, keepdims=True))
        if moves:
            moves.wait(also=(i == nb - 1))

    row = lambda w: pl.BlockSpec((tm, w), lambda i: (i, 0))
    full = lambda a: pl.BlockSpec(a.shape, lambda i: (0, 0))
    ex_in = exchange.operands if exchange else []
    ex_out = exchange.out_shapes if exchange else []
    res = pl.pallas_call(
        body, name="inproj_bwd", grid=(nb,),
        in_specs=[row(SEC), row(SEC), row(GATE_COLS), full(w_a), full(w_b), full(w_g), row(d), row(d), full(g)]
                 + [ANY] * len(ex_in),
        out_specs=[row(d), pl.BlockSpec((1, d), lambda i: (0, 0))] + [ANY] * len(ex_out),
        out_shape=[jax.ShapeDtypeStruct((s, d), F32), jax.ShapeDtypeStruct((1, d), F32)] + ex_out,
        scratch_shapes=exchange.scratch() if exchange else [],
        compiler_params=_params("arbitrary"),
    )(du_a, du_b, du_g, w_a, w_b, w_g, x, dx2, g, *ex_in)
    return res[0], res[1], list(res[2:])


def _rwkv_elementwise(ua, prev_row, first, mu, wl, w0, a0, kkw, kaw, bd):
    tm = ua.shape[0]
    rows = lax.broadcasted_iota(jnp.int32, (tm, 1), 0)
    prev = jnp.where(first, jnp.zeros_like(prev_row), prev_row)
    shifted = jnp.where(rows == 0, prev, pltpu.roll(ua, 1, 0))
    delta = shifted - ua
    us = ua + delta * mu
    r = us[:, 0:512]
    k0 = us[:, 512:1024]
    v = us[:, 1024:1536]
    lo = us[:, 1536:1664]
    gate = us[:, 1664:2176]
    lane = lax.broadcasted_iota(jnp.int32, (1, LANES), 1)
    th = jnp.tanh(lo)
    lin = jnp.where(lane < LORA, th, lo)
    ll = jnp.dot(lin.astype(BF16), wl, preferred_element_type=F32)
    sz = _sigmoid(w0 + ll[:, :512])
    e = sz * math.exp(-0.5)
    dec = jnp.exp(-e)
    a = _sigmoid(a0 + ll[:, 512:])
    kk0 = k0 * kkw
    ss = _head_sum(kk0 * kk0, bd)
    nrm = jnp.maximum(jnp.sqrt(ss), 1e-12)
    kk = kk0 / nrm
    k = k0 * (1.0 + (a - 1.0) * kaw)
    return dict(delta=delta, us=us, r=r, k0=k0, v=v, lo=lo, gate=gate, th=th, lin=lin, sz=sz, e=e, dec=dec,
                a=a, kk0=kk0, ss=ss, nrm=nrm, kk=kk, k=k)


def _rwkv_front(x, g, w_a, mu, wl, w0, a0, kkw, kaw, tm=256):
    s, d = x.shape

    def body(x_ref, g_ref, wa_ref, mu_ref, wl_ref, w0_ref, a0_ref, kkw_ref, kaw_ref,
             h_ref, ua_ref, r_ref, w_ref, k_ref, v_ref, a_ref, b_ref, gate_ref, last_row):
        i = pl.program_id(0)
        xv = x_ref[...]
        h = (xv * lax.rsqrt(jnp.mean(xv * xv, axis=-1, keepdims=True) + RMS_EPS) * g_ref[...]).astype(BF16)
        h_ref[...] = h
        ua = jnp.dot(h, wa_ref[...], preferred_element_type=F32)
        ua_ref[...] = ua

        @pl.when(i == 0)
        def _():
            last_row[...] = jnp.zeros_like(last_row)

        f = _rwkv_elementwise(ua, last_row[...], i == 0, mu_ref[...], wl_ref[...], w0_ref[...],
                              a0_ref[...], kkw_ref[...], kaw_ref[...], _head_ones())
        last_row[...] = ua[tm - 1:tm, :]
        r_ref[...] = f["r"]
        w_ref[...] = f["dec"]
        k_ref[...] = f["k"]
        v_ref[...] = f["v"]
        a_ref[...] = -f["kk"]
        b_ref[...] = f["kk"] * f["a"]
        gate_ref[...] = f["gate"]

    vec = lambda w: pl.BlockSpec((1, w), lambda i: (0, 0))
    row = lambda w: pl.BlockSpec((tm, w), lambda i: (i, 0))
    return pl.pallas_call(
        body, name="rwkv_front", grid=(s // tm,),
        in_specs=[row(d), vec(d), pl.BlockSpec(w_a.shape, lambda i: (0, 0), pipeline_mode=pl.Buffered(1)),
                  vec(SEC), pl.BlockSpec((LANES, 2 * D_HALF), lambda i: (0, 0)),
                  vec(D_HALF), vec(D_HALF), vec(D_HALF), vec(D_HALF)],
        out_specs=[row(d), row(SEC)] + [row(D_HALF)] * 7,
        out_shape=[jax.ShapeDtypeStruct((s, d), BF16), jax.ShapeDtypeStruct((s, SEC), F32)]
                  + [jax.ShapeDtypeStruct((s, D_HALF), F32)] * 7,
        scratch_shapes=[pltpu.VMEM((1, SEC), F32)],
        compiler_params=_params("arbitrary"),
    )(x, g, w_a, mu, wl, w0, a0, kkw, kaw)


SCAN_TB = 128
N_PAIRS = 4


def _pair_sum(x, left):
    s_l = jnp.sum(jnp.where(left, x, 0.0), axis=1, keepdims=True)
    s_r = jnp.sum(jnp.where(left, 0.0, x), axis=1, keepdims=True)
    return jnp.where(left, s_l, s_r)


def _pair_dot(x, row_l, row_r, left):
    s_l = jnp.sum(x * row_l, axis=1, keepdims=True)
    s_r = jnp.sum(x * row_r, axis=1, keepdims=True)
    return jnp.where(left, s_l, s_r)


def _halves(rows8):
    lane = lax.broadcasted_iota(jnp.int32, rows8.shape, 1)
    keep_left = (lane & (LANES - 1)) < HEAD
    return jnp.where(keep_left, rows8, 0.0), jnp.where(keep_left, 0.0, rows8)


def _quad_consts():
    lane = lax.broadcasted_iota(jnp.int32, (HEAD, 2 * LANES), 1)
    rowi = lax.broadcasted_iota(jnp.int32, (HEAD, 2 * LANES), 0)
    diag2 = rowi == (lane & (HEAD - 1))
    r = lax.broadcasted_iota(jnp.int32, (2 * LANES, 2 * LANES), 0) >> 6
    c = lax.broadcasted_iota(jnp.int32, (2 * LANES, 2 * LANES), 1) >> 6
    return diag2, (r == c).astype(BF16)


def _rows_to_columns(x8, diag2, bd2):
    lhs = jnp.concatenate([jnp.where(diag2, x8[i:i + 1], 0.0).astype(BF16) for i in range(SUBLANES)], axis=0)
    return jnp.dot(lhs, bd2, preferred_element_type=F32)


def _diag_rows(qtile, diag2, bd2, sub_row2):
    res = jnp.dot(qtile, bd2, preferred_element_type=F32)
    out = jnp.zeros((SUBLANES, 2 * LANES), F32)
    for i in range(SUBLANES):
        out = jnp.where(sub_row2 == i, _colsum(jnp.where(diag2, res[i * HEAD:(i + 1) * HEAD], 0.0)), out)
    return out


def _store_tile(qbuf, slot, p, i, x):
    qbuf[slot, p // 2, i * HEAD:(i + 1) * HEAD, (p % 2) * LANES:(p % 2 + 1) * LANES] = x.astype(BF16)


def _left_half():
    return lax.broadcasted_iota(jnp.int32, (HEAD, LANES), 1) < HEAD


def _split_refs(refs, n_rows, n_out, exchange):
    n_in = len(exchange.operands) if exchange else 0
    n_ex_out = len(exchange.out_shapes) if exchange else 0
    refs = list(refs)
    rows, refs = refs[:n_rows], refs[n_rows:]
    ex_in, refs = refs[:n_in], refs[n_in:]
    outs, refs = refs[:n_out], refs[n_out:]
    ex_out, refs = refs[:n_ex_out], refs[n_ex_out:]
    scratch, sems = (refs[:-3], refs[-3:]) if exchange else (refs, None)
    moves = exchange.moves(ex_in, ex_out, sems) if exchange else None
    return rows, outs, scratch, moves


def _wkv_fwd(r, w, k, a, b, v, exchange=None):
    s = r.shape[0]
    tb = SCAN_TB
    nb = s // tb

    def body(*refs):
        (r_ref, w_ref, k_ref, a_ref, b_ref, v_ref), (y_ref, st_ref), (state, vbuf, qbuf), moves = _split_refs(
            refs, 6, 2, exchange)
        g = pl.program_id(0)
        if moves:
            moves.start(also=(g == 0))

        @pl.when(g == 0)
        def _():
            state[...] = jnp.zeros_like(state)
            qbuf[...] = jnp.zeros_like(qbuf)

        left = _left_half()
        diag2, bd2 = _quad_consts()
        sub_row2 = lax.broadcasted_iota(jnp.int32, (SUBLANES, 2 * LANES), 0)
        groups = tb // SUBLANES
        quads = [slice(g2 * 2 * LANES, (g2 + 1) * 2 * LANES) for g2 in range(2)]

        def rows_of(q):
            return pl.ds(pl.multiple_of(q * SUBLANES, SUBLANES), SUBLANES)

        def v_tiles(q, slot):
            v8 = v_ref[rows_of(q), :]
            for g2 in range(2):
                vbuf[slot, g2] = _rows_to_columns(v8[:, quads[g2]], diag2, bd2)

        def chain(q, slot):
            rows8 = rows_of(q)
            a8, w8, b8, k8, r8 = (x[rows8, :] for x in (a_ref, w_ref, b_ref, k_ref, r_ref))
            pairs = [slice(p * LANES, (p + 1) * LANES) for p in range(N_PAIRS)]
            a_next = pltpu.roll(a8, SUBLANES - 1, 0)
            (a8_l, a8_r), (wa8_l, wa8_r) = _halves(a8), _halves(w8 * a_next)
            ba8 =jnp.concatenate([_pair_sum(b8[:, pr] * a_next[:, pr], left[0:SUBLANES]) for pr in pairs], axis=1)
            ka8 = jnp.concatenate([_pair_sum(k8[:, pr] * a_next[:, pr], left[0:SUBLANES]) for pr in pairs], axis=1)
            sp = [state[p] for p in range(N_PAIRS)]
            for i in range(0, SUBLANES, 2):
                r0, r1 = slice(i, i + 1), slice(i + 1, i + 2)
                sums = [(_pair_dot(sp[p], a8_l[r0, pairs[p]], a8_r[r0, pairs[p]], left),
                         _pair_dot(sp[p], wa8_l[r0, pairs[p]], wa8_r[r0, pairs[p]], left)) for p in range(N_PAIRS)]
                sa0, sa1 = [s[0] for s in sums], [s[1] for s in sums]
                for p in range(N_PAIRS):
                    pr = pairs[p]
                    inner = slice((p % 2) * LANES, (p % 2 + 1) * LANES)
                    vt0 = vbuf[slot, p // 2, i * HEAD:(i + 1) * HEAD, inner]
                    vt1 = vbuf[slot, p // 2, (i + 1) * HEAD:(i + 2) * HEAD, inner]
                    sa_next = sa1[p] + sa0[p] * ba8[r0, pr] + vt0 * ka8[r0, pr]
                    s1 = sp[p] * w8[r0, pr] + sa0[p] * b8[r0, pr] + vt0 * k8[r0, pr]
                    st_ref[q * SUBLANES + i, p] = s1
                    _store_tile(qbuf, slot, p, i, s1 * r8[r0, pr])
                    s2 = s1 * w8[r1, pr] + sa_next * b8[r1, pr] + vt1 * k8[r1, pr]
                    st_ref[q * SUBLANES + i + 1, p] = s2
                    _store_tile(qbuf, slot, p, i + 1, s2 * r8[r1, pr])
                    sp[p] = s2
            for p in range(N_PAIRS):
                state[p] = sp[p]

        def y_rows(q, slot):
            for g2 in range(2):
                y_ref[rows_of(q), quads[g2]] = _diag_rows(qbuf[slot, g2], diag2, bd2, sub_row2)

        v_tiles(0, 0)

        def two_groups(j, carry):
            q0 = 2 * j
            v_tiles(q0 + 1, 1)
            chain(q0, 0)
            y_rows(jnp.maximum(q0 - 1, 0), 1)
            v_tiles(jnp.minimum(q0 + 2, groups - 1), 0)
            chain(q0 + 1, 1)
            y_rows(q0, 0)
            return carry

        lax.fori_loop(0, groups // 2, two_groups, 0)
        y_rows(groups - 1, 1)
        if moves:
            moves.wait(also=(g == nb - 1))

    rows = pl.BlockSpec((tb, D_HALF), lambda g: (g, 0))
    ex_in = exchange.operands if exchange else []
    ex_out = exchange.out_shapes if exchange else []
    res = pl.pallas_call(
        body, name="wkv_fwd", grid=(nb,),
        in_specs=[rows] * 6 + [ANY] * len(ex_in),
        out_specs=[rows, pl.BlockSpec((tb, N_PAIRS, HEAD, LANES), lambda g: (g, 0, 0, 0))] + [ANY] * len(ex_out),
        out_shape=[jax.ShapeDtypeStruct((s, D_HALF), F32),
                   jax.ShapeDtypeStruct((s, N_PAIRS, HEAD, LANES), F32)] + ex_out,
        scratch_shapes=[pltpu.VMEM((N_PAIRS, HEAD, LANES), F32),
                        pltpu.VMEM((2, 2, SUBLANES * HEAD, 2 * LANES), F32),
                        pltpu.VMEM((2, 2, SUBLANES * HEAD, 2 * LANES), BF16)]
                       + (exchange.scratch() if exchange else []),
        compiler_params=_params("arbitrary"),
    )(r, w, k, a, b, v, *ex_in)
    return res[0], res[1], list(res[2:])


def _wkv_bwd(r, w, k, a, b, v, dy, st, exchange=None):
    s = r.shape[0]
    tb = SCAN_TB
    nb = s // tb

    def body(*refs):
        ((r_ref, w_ref, k_ref, a_ref, b_ref, v_ref, dy_ref, st_ref, before_ref),
         (dr_ref, dw_ref, dk_ref, dv_ref, da_ref, db_ref), (dstate, vbuf, qbuf, sbuf),
         moves) = _split_refs(refs, 9, 6, exchange)
        g = pl.program_id(0)
        first_block = g == nb - 1
        if moves:
            moves.start(also=(g == 0))

        @pl.when(g == 0)
        def _():
            dstate[...] = jnp.zeros_like(dstate)
            qbuf[...] = jnp.zeros_like(qbuf)

        left = _left_half()
        diag2, bd2 = _quad_consts()
        sub_row = lax.broadcasted_iota(jnp.int32, (SUBLANES, LANES), 0)
        sub_row2 = lax.broadcasted_iota(jnp.int32, (SUBLANES, 2 * LANES), 0)
        groups = tb // SUBLANES
        quads = [slice(g2 * 2 * LANES, (g2 + 1) * 2 * LANES) for g2 in range(2)]
        row_refs = (dr_ref, dw_ref, dk_ref, da_ref, db_ref)

        def rows_of(q):
            return pl.ds(pl.multiple_of(q * SUBLANES, SUBLANES), SUBLANES)

        def state_before(q, i, p):
            if i > 0:
                return st_ref[q * SUBLANES + i - 1, p]
            return jnp.where(q == 0, jnp.where(first_block, 0.0, before_ref[0, p]),
                             st_ref[jnp.maximum(q * SUBLANES - 1, 0), p])

        def column_tiles(q, slot):
            rows8 = rows_of(q)
            for kind, ref in enumerate((v_ref, dy_ref)):
                x8 = ref[rows8, :]
                for g2 in range(2):
                    vbuf[slot, kind, g2] = _rows_to_columns(x8[:, quads[g2]], diag2, bd2)
            a8 = a_ref[rows8, :]
            for i in range(SUBLANES):
                for p in range(N_PAIRS):
                    _store_tile(sbuf, 0, p, i, state_before(q, i, p) * a8[i:i + 1, p * LANES:(p + 1) * LANES])
            for g2 in range(2):
                vbuf[slot, 2, g2] = jnp.dot(sbuf[0, g2], bd2, preferred_element_type=F32)

        def chain(q, slot):
            rows8 = rows_of(q)
            a8, w8, b8, k8, r8 = (x[rows8, :] for x in (a_ref, w_ref, b_ref, k_ref, r_ref))
            b8_l, b8_r = _halves(b8)
            dsp = [dstate[p] for p in range(N_PAIRS)]
            outs = [[jnp.zeros((SUBLANES, LANES), F32) for _ in row_refs] for _ in range(N_PAIRS)]
            after = [st_ref[q * SUBLANES + SUBLANES - 1, p] for p in range(N_PAIRS)]
            for i in reversed(range(SUBLANES)):
                row = slice(i, i + 1)
                pl_ = [slice(p * LANES, (p + 1) * LANES) for p in range(N_PAIRS)]
                tile = [(p // 2, slice(i * HEAD, (i + 1) * HEAD), slice((p % 2) * LANES, (p % 2 + 1) * LANES))
                        for p in range(N_PAIRS)]
                sp = [state_before(q, i, p) for p in range(N_PAIRS)]
                dyt = [vbuf[(slot, 1) + tile[p]] for p in range(N_PAIRS)]
                ds = [dsp[p] + dyt[p] * r8[row, pl_[p]] for p in range(N_PAIRS)]
                dsa = [_pair_dot(ds[p], b8_l[row, pl_[p]], b8_r[row, pl_[p]], left) for p in range(N_PAIRS)]
                sa = [vbuf[(slot, 2) + tile[p]] for p in range(N_PAIRS)]
                for p in range(N_PAIRS):
                    ar, wr, br, kr = (x[row, pl_[p]] for x in (a8, w8, b8, k8))
                    vt = vbuf[(slot, 0) + tile[p]]
                    dsp[p] = ds[p] * wr + dsa[p] * ar
                    new = (_colsum(after[p] * dyt[p]), _colsum(ds[p] * sp[p]), _colsum(ds[p] * vt),
                           _colsum(sp[p] * dsa[p]), _colsum(ds[p] * sa[p]))
                    outs[p] = [jnp.where(sub_row == i, n, o) for n, o in zip(new, outs[p])]
                    _store_tile(qbuf, slot, p, i, ds[p] * kr)
                after = sp
            for p in range(N_PAIRS):
                dstate[p] = dsp[p]
                for ref, o in zip(row_refs, outs[p]):
                    ref[rows8, p * LANES:(p + 1) * LANES] = o

        def dv_rows(q, slot):
            for g2 in range(2):
                dv_ref[rows_of(q), quads[g2]] = _diag_rows(qbuf[slot, g2], diag2, bd2, sub_row2)

        column_tiles(groups - 1, 0)

        def two_groups(j, carry):
            q0 = groups - 1 - 2 * j
            column_tiles(q0 - 1, 1)
            chain(q0, 0)
            dv_rows(jnp.minimum(q0 + 1, groups - 1), 1)
            column_tiles(jnp.maximum(q0 - 2, 0), 0)
            chain(q0 - 1, 1)
            dv_rows(q0, 0)
            return carry

        lax.fori_loop(0, groups // 2, two_groups, 0)
        dv_rows(0, 1)
        if moves:
            moves.wait(also=(g == nb - 1))

    rows = pl.BlockSpec((tb, D_HALF), lambda g: (nb - 1 - g, 0))
    ex_in = exchange.operands if exchange else []
    ex_out = exchange.out_shapes if exchange else []
    res = pl.pallas_call(
        body, name="wkv_bwd", grid=(nb,),
        in_specs=[rows] * 7 + [pl.BlockSpec((tb, N_PAIRS, HEAD, LANES), lambda g: (nb - 1 - g, 0, 0, 0)),
                               pl.BlockSpec((1, N_PAIRS, HEAD, LANES),
                                            lambda g: (jnp.maximum((nb - 1 - g) * tb - 1, 0), 0, 0, 0))]
                 + [ANY] * len(ex_in),
        out_specs=[rows] * 6 + [ANY] * len(ex_out),
        out_shape=[jax.ShapeDtypeStruct((s, D_HALF), F32)] * 6 + ex_out,
        scratch_shapes=[pltpu.VMEM((N_PAIRS, HEAD, LANES), F32),
                        pltpu.VMEM((2, 3, 2, SUBLANES * HEAD, 2 * LANES), F32),
                        pltpu.VMEM((2, 2, SUBLANES * HEAD, 2 * LANES), BF16),
                        pltpu.VMEM((1, 2, SUBLANES * HEAD, 2 * LANES), BF16)]
                       + (exchange.scratch() if exchange else []),
        compiler_params=_params("arbitrary"),
    )(r, w, k, a, b, v, dy, st, st, *ex_in)
    return list(res[:6]), list(res[6:])


def _rwkv_post_math(y, r, k, v, gate, lw, lb, rk, bd):
    mean = _head_sum(y, bd) * (1.0 / HEAD)
    yc = y - mean
    var = _head_sum(yc * yc, bd) * (1.0 / HEAD)
    rstd = lax.rsqrt(var + LNX_EPS)
    yn = yc * rstd
    rkk = _head_sum(r * k * rk, bd)
    sg = _sigmoid(gate)
    pre = yn * lw + lb + rkk * v
    return yn, rstd, rkk, sg, pre


def _rwkv_prep_bwd(u_a, h_t, grads, mu, wl, w0, a0, kkw, kaw, tm=256):
    s = u_a.shape[0]
    nb = s // tm
    d = h_t.shape[0]

    def body(ua_ref, prev_ref, ht_ref, drs_ref, dws_ref, dks_ref, dvs_ref, das_ref, dbs_ref, drb_ref, dkb_ref, dvb_ref,
             dgt_ref, mu_ref, wl_ref, w0_ref, a0_ref, kkw_ref, kaw_ref,
             du_ref, dwa_ref, dmu_ref, dwl_ref, dw0_ref, da0_ref, dkkw_ref, dkaw_ref, carry):
        i = pl.program_id(0)

        @pl.when(i == 0)
        def _():
            carry[...] = jnp.zeros_like(carry)
            for ref in (dwa_ref, dmu_ref, dwl_ref, dw0_ref, da0_ref, dkkw_ref, dkaw_ref):
                ref[...] = jnp.zeros_like(ref)

        bd = _head_ones()
        mu_v, wl_v, kkw_v, kaw_v = mu_ref[...], wl_ref[...], kkw_ref[...], kaw_ref[...]
        f = _rwkv_elementwise(ua_ref[...], prev_ref[7:8, :], i == nb - 1, mu_v, wl_v, w0_ref[...],
                              a0_ref[...], kkw_v, kaw_v, bd)
        a, kk, k0 = f["a"], f["kk"], f["k0"]
        dk = dks_ref[...] + dkb_ref[...]
        dbs = dbs_ref[...]
        dkk = dbs * a - das_ref[...]
        da = dbs * kk + dk * k0 * kaw_v
        dk0 = dk * (1.0 + (a - 1.0) * kaw_v)
        dkaw_ref[...] += _colsum(dk * k0 * (a - 1.0))
        inv = 1.0 / f["nrm"]
        proj = _head_sum(dkk * kk, bd)
        dkk0 = jnp.where(f["ss"] > 1e-24, (dkk - kk * proj) * inv, dkk * inv)
        dk0 = dk0 + dkk0 * kkw_v
        dkkw_ref[...] += _colsum(dkk0 * k0)
        dza = da * a * (1.0 - a)
        da0_ref[...] += _colsum(dza)
        dz = -dws_ref[...] * f["dec"] * f["e"] * (1.0 - f["sz"])
        dw0_ref[...] += _colsum(dz)
        dll = jnp.concatenate([dz, dza], axis=1).astype(BF16)
        dwl_ref[...] += _dot_tn(f["lin"].astype(BF16), dll)
        dlin = _dot_nt(dll, wl_v)
        lane = lax.broadcasted_iota(jnp.int32, (1, LANES), 1)
        th = f["th"]
        dlo = jnp.where(lane < LORA, dlin * (1.0 - th * th), dlin)
        dus = jnp.concatenate([drs_ref[...] + drb_ref[...], dk0, dvs_ref[...] + dvb_ref[...], dlo, dgt_ref[...]],
                              axis=1)
        dmu_ref[...] += _colsum(dus * f["delta"])
        g1 = dus * mu_v
        rows = lax.broadcasted_iota(jnp.int32, (tm, 1), 0)
        up = jnp.where(rows == tm - 1, carry[...], pltpu.roll(g1, tm - 1, 0))
        dua = dus - g1 + up
        du_ref[...] = dua
        dwa_ref[...] += jnp.dot(ht_ref[...], dua.astype(BF16), preferred_element_type=F32)
        carry[...] = g1[0:1, :]

    rev = lambda w: pl.BlockSpec((tm, w), lambda i: (nb - 1 - i, 0))
    vec = lambda w: pl.BlockSpec((1, w), lambda i: (0, 0))
    wl_spec = pl.BlockSpec((LANES, 2 * D_HALF), lambda i: (0, 0))
    return pl.pallas_call(
        body, name="rwkv_prep_bwd", grid=(nb,),
        in_specs=[rev(SEC), pl.BlockSpec((8, SEC), lambda i: (jnp.maximum((nb - 1 - i) * (tm // 8) - 1, 0), 0)),
                  pl.BlockSpec((d, tm), lambda i: (0, nb - 1 - i))]
                 + [rev(D_HALF)] * 10 + [vec(SEC), wl_spec] + [vec(D_HALF)] * 4,
        out_specs=[rev(SEC), pl.BlockSpec((d, SEC), lambda i: (0, 0)), vec(SEC), wl_spec] + [vec(D_HALF)] * 4,
        out_shape=[jax.ShapeDtypeStruct((s, SEC), F32), jax.ShapeDtypeStruct((d, SEC), F32),
                   jax.ShapeDtypeStruct((1, SEC), F32),
                   jax.ShapeDtypeStruct((LANES, 2 * D_HALF), F32)] + [jax.ShapeDtypeStruct((1, D_HALF), F32)] * 4,
        scratch_shapes=[pltpu.VMEM((1, SEC), F32)],
        compiler_params=_params("arbitrary"),
    )(u_a, u_a, h_t, *grads, mu, wl, w0, a0, kkw, kaw)


def _tri(tm, lower):
    r = lax.broadcasted_iota(jnp.int32, (tm, tm), 0)
    c = lax.broadcasted_iota(jnp.int32, (tm, tm), 1)
    return ((r >= c) if lower else (r <= c)).astype(BF16)


def _head_rms(x, g, bd):
    rinv = lax.rsqrt(_head_sum(x * x, bd) * (1.0 / HEAD) + RMS_EPS)
    xh = x * rinv
    return xh, rinv, xh * g


def _fox_front(h, w_b, fb, qg, kg, tm=256):
    s, d = h.shape

    def body(h_ref, wb_ref, fb_ref, qg_ref, kg_ref, ub_ref, q_ref, k_ref, v_ref, cc_ref, cr_ref, carry):
        i = pl.program_id(0)

        @pl.when(i == 0)
        def _():
            carry[...] = jnp.zeros_like(carry)

        ub_ref[...] = jnp.dot(h_ref[...], wb_ref[...], preferred_element_type=F32)
        bd = _head_ones()
        _, _, qn = _head_rms(ub_ref[:, 0:512], qg_ref[...], bd)
        _, _, kn = _head_rms(ub_ref[:, 512:1024], kg_ref[...], bd)
        q_ref[...] = (qn * ATT_SCALE).astype(BF16)
        k_ref[...] = kn.astype(BF16)
        v_ref[...] = ub_ref[:, 1024:1536].astype(BF16)
        lane = lax.broadcasted_iota(jnp.int32, (1, LANES), 1)
        logf = jnp.where(lane < N_HEADS, _log_sigmoid(ub_ref[:, 2048:2176] + fb_ref[...]), 0.0)
        cum = _exact_dot(logf, _tri(tm, True), ones_first=True) + carry[...]
        for h in range(N_HEADS):
            cc_ref[h] = jnp.broadcast_to(cum[:, h:h + 1], (tm, LANES))
        cr_ref[...] = jnp.transpose(cum)[0:N_HEADS, :]
        carry[...] = cum[tm - 1:tm, :]

    blk = pl.BlockSpec((tm, D_HALF), lambda i: (i, 0))
    return pl.pallas_call(
        body, name="fox_front", grid=(s // tm,),
        in_specs=[pl.BlockSpec((tm, d), lambda i: (i, 0)),
                  pl.BlockSpec(w_b.shape, lambda i: (0, 0), pipeline_mode=pl.Buffered(1)),
                  pl.BlockSpec((1, LANES), lambda i: (0, 0)),
                  pl.BlockSpec((1, D_HALF), lambda i: (0, 0)), pl.BlockSpec((1, D_HALF), lambda i: (0, 0))],
        out_specs=[pl.BlockSpec((tm, SEC), lambda i: (i, 0)), blk, blk, blk,
                   pl.BlockSpec((N_HEADS, tm, LANES), lambda i: (0, i, 0)), pl.BlockSpec((N_HEADS, tm), lambda i: (0, i))],
        out_shape=[jax.ShapeDtypeStruct((s, SEC), F32)] + [jax.ShapeDtypeStruct((s, D_HALF), BF16)] * 3
                  + [jax.ShapeDtypeStruct((N_HEADS, s, LANES), F32), jax.ShapeDtypeStruct((N_HEADS, s), F32)],
        scratch_shapes=[pltpu.VMEM((1, LANES), F32)],
        compiler_params=_params("arbitrary"),
    )(h, w_b, fb, qg, kg)


ATT_T = 256


def _tiles(nblk, by_query):
    if by_query:
        pairs = [(i, j) for i in range(nblk) for j in range(i + 1)]
    else:
        pairs = [(i, j) for j in range(nblk) for i in range(j, nblk)]
    return (jnp.asarray([p[0] for p in pairs], jnp.int32), jnp.asarray([p[1] for p in pairs], jnp.int32))


def _attn_fwd(q, k, v, cc, cr):
    s = q.shape[0]
    t = ATT_T
    nblk = s // t

    def body(qi_ref, kj_ref, q_ref, k_ref, v_ref, cc_ref, cr_ref, o_ref, lse_ref, m_sc, l_sc, acc_sc):
        i = qi_ref[pl.program_id(0)]
        j = kj_ref[pl.program_id(0)]

        @pl.when(j == 0)
        def _():
            m_sc[...] = jnp.full_like(m_sc, NEG)
            l_sc[...] = jnp.zeros_like(l_sc)
            acc_sc[...] = jnp.zeros_like(acc_sc)

        def tile(on_diagonal):
            causal = _causal_tile(t) if on_diagonal else None
            left = lax.broadcasted_iota(jnp.int32, (1, LANES), 1) < HEAD
            for p in range(N_PAIRS):
                lanes = slice(p * LANES, (p + 1) * LANES)
                q2, k2, v2 = q_ref[:, lanes], k_ref[:, lanes], v_ref[:, lanes]
                acc2 = acc_sc[:, lanes]
                for e in range(2):
                    h = 2 * p + e
                    msk = left if e == 0 else jnp.logical_not(left)
                    sc = _dot_nt(jnp.where(msk, q2, jnp.zeros_like(q2)), k2)
                    sc = sc + (_wide(cc_ref[h]) - cr_ref[h:h + 1, :])
                    if on_diagonal:
                        sc = jnp.where(causal, sc, NEG)
                    m_prev = m_sc[h]
                    m_new = jnp.maximum(m_prev, jnp.max(sc, axis=1, keepdims=True))
                    alpha = jnp.exp(m_prev - m_new)
                    pm = jnp.exp(sc - _wide(m_new))
                    l_sc[h] = alpha * l_sc[h] + jnp.sum(pm, axis=1, keepdims=True)
                    m_sc[h] = m_new
                    pv = jnp.dot(pm.astype(BF16), v2, preferred_element_type=F32)
                    acc2 = jnp.where(msk, alpha * acc2 + pv, acc2)
                acc_sc[:, lanes] = acc2

        pl.when(j < i)(functools.partial(tile, False))
        pl.when(j == i)(functools.partial(tile, True))

        @pl.when(j == i)
        def _():
            left = lax.broadcasted_iota(jnp.int32, (1, LANES), 1) < HEAD
            for p in range(N_PAIRS):
                lanes = slice(p * LANES, (p + 1) * LANES)
                inv = jnp.where(left, 1.0 / l_sc[2 * p], 1.0 / l_sc[2 * p + 1])
                o_ref[:, lanes] = acc_sc[:, lanes] * inv
            for h in range(N_HEADS):
                lse_ref[h] = m_sc[h] + jnp.log(l_sc[h])

    qi, kj = _tiles(nblk, by_query=True)
    qblk = pl.BlockSpec((t, D_HALF), lambda n, qi, kj: (qi[n], 0))
    kblk = pl.BlockSpec((t, D_HALF), lambda n, qi, kj: (kj[n], 0))
    qrep = pl.BlockSpec((N_HEADS, t, LANES), lambda n, qi, kj: (0, qi[n], 0))
    return pl.pallas_call(
        body, name="fox_attn_fwd",
        grid_spec=pltpu.PrefetchScalarGridSpec(
            num_scalar_prefetch=2, grid=(qi.shape[0],),
            in_specs=[qblk, kblk, kblk, qrep, pl.BlockSpec((N_HEADS, t), lambda n, qi, kj: (0, kj[n]))],
            out_specs=[qblk, qrep],
            scratch_shapes=[pltpu.VMEM((N_HEADS, t, LANES), F32), pltpu.VMEM((N_HEADS, t, LANES), F32),
                            pltpu.VMEM((t, D_HALF), F32)]),
        out_shape=[jax.ShapeDtypeStruct((s, D_HALF), F32), jax.ShapeDtypeStruct((N_HEADS, s, LANES), F32)],
        compiler_params=_params("arbitrary"),
    )(qi, kj, q, k, v, cc, cr)


def _causal_tile(t):
    return lax.broadcasted_iota(jnp.int32, (t, t), 0) >= lax.broadcasted_iota(jnp.int32, (t, t), 1)


def _wide(x):
    return jnp.concatenate([x, x], axis=1)


def _attn_probs(q2, k2, v2, do2, msk, causal, bias, lse_rows):
    zero = jnp.zeros_like(q2)
    qh = jnp.where(msk, q2, zero)
    doh = jnp.where(msk, do2, zero)
    sc = _dot_nt(qh, k2) + bias
    if causal is not None:
        sc = jnp.where(causal, sc, NEG)
    pm = jnp.exp(sc - _wide(lse_rows))
    dp = _dot_nt(doh, v2)
    return qh, doh, pm, dp


def _attn_bwd_rowdot(q, k, v, do, lse, cc, cr):
    s = q.shape[0]
    t = ATT_T
    nblk = s // t

    def body(qi_ref, kj_ref, q_ref, k_ref, v_ref, do_ref, lse_ref, cc_ref, cr_ref, dd_ref, acc):
        i = qi_ref[pl.program_id(0)]
        j = kj_ref[pl.program_id(0)]

        @pl.when(j == 0)
        def _():
            acc[...] = jnp.zeros_like(acc)

        def tile(on_diagonal):
            causal = _causal_tile(t) if on_diagonal else None
            left = lax.broadcasted_iota(jnp.int32, (1, LANES), 1) < HEAD
            for p in range(N_PAIRS):
                lanes = slice(p * LANES, (p + 1) * LANES)
                q2, k2, v2, do2 = q_ref[:, lanes], k_ref[:, lanes], v_ref[:, lanes], do_ref[:, lanes]
                for e in range(2):
                    h = 2 * p + e
                    msk = left if e == 0 else jnp.logical_not(left)
                    bias = _wide(cc_ref[h]) - cr_ref[h:h + 1, :]
                    _, _, pm, dp = _attn_probs(q2, k2, v2, do2, msk, causal, bias, lse_ref[h])
                    acc[h] += jnp.sum(pm * dp, axis=1, keepdims=True)

        pl.when(j < i)(functools.partial(tile, False))
        pl.when(j == i)(functools.partial(tile, True))

        @pl.when(j == i)
        def _():
            dd_ref[...] = acc[...]

    qi, kj = _tiles(nblk, by_query=True)
    qblk = pl.BlockSpec((t, D_HALF), lambda n, qi, kj: (qi[n], 0))
    qcol = pl.BlockSpec((N_HEADS, t, LANES), lambda n, qi, kj: (0, qi[n], 0))
    kblk = pl.BlockSpec((t, D_HALF), lambda n, qi, kj: (kj[n], 0))
    return pl.pallas_call(
        body, name="fox_attn_rowdot",
        grid_spec=pltpu.PrefetchScalarGridSpec(
            num_scalar_prefetch=2, grid=(qi.shape[0],),
            in_specs=[qblk, kblk, kblk, qblk, qcol, qcol, pl.BlockSpec((N_HEADS, t), lambda n, qi, kj: (0, kj[n]))],
            out_specs=qcol, scratch_shapes=[pltpu.VMEM((N_HEADS, t, LANES), F32)]),
        out_shape=jax.ShapeDtypeStruct((N_HEADS, s, LANES), F32),
        compiler_params=_params("arbitrary"),
    )(qi, kj, q, k, v, do, lse, cc, cr)


def _attn_bwd(q, k, v, do, lse, dd, cc, cr):
    s = q.shape[0]
    t = ATT_T
    nblk = s // t

    def body(qi_ref, kj_ref, q_ref, k_ref, v_ref, do_ref, lse_ref, dd_ref, cc_ref, cr_ref,
             dq_ref, dk_ref, dv_ref, dcr_ref, dk_sc, dv_sc, dcr_sc):
        i = qi_ref[pl.program_id(0)]
        j = kj_ref[pl.program_id(0)]

        @pl.when(pl.program_id(0) == 0)
        def _():
            dq_ref[...] = jnp.zeros_like(dq_ref)

        @pl.when(i == j)
        def _():
            dk_sc[...] = jnp.zeros_like(dk_sc)
            dv_sc[...] = jnp.zeros_like(dv_sc)
            dcr_sc[...] = jnp.zeros_like(dcr_sc)

        def tile(on_diagonal):
            causal = _causal_tile(t) if on_diagonal else None
            left = lax.broadcasted_iota(jnp.int32, (1, LANES), 1) < HEAD
            qrows = pl.ds(pl.multiple_of(i * t, t), t)
            for p in range(N_PAIRS):
                lanes = slice(p * LANES, (p + 1) * LANES)
                q2, k2, v2, do2 = q_ref[:, lanes], k_ref[:, lanes], v_ref[:, lanes], do_ref[:, lanes]
                zero = jnp.zeros_like(q2)
                dq2 = jnp.zeros((t, LANES), F32)
                dk2 = jnp.zeros((t, LANES), F32)
                dv2 = jnp.zeros((t, LANES), F32)
                for e in range(2):
                    h = 2 * p + e
                    msk = left if e == 0 else jnp.logical_not(left)
                    bias = _wide(cc_ref[h]) - cr_ref[h:h + 1, :]
                    qh, doh, pm, dp = _attn_probs(q2, k2, v2, do2, msk, causal, bias, lse_ref[h])
                    dsc = pm * (dp - _wide(dd_ref[h]))
                    dsb = dsc.astype(BF16)
                    dv2 += _dot_tn(pm.astype(BF16), doh)
                    dk2 += _dot_tn(dsb, qh)
                    dq2 += jnp.dot(dsb, jnp.where(msk, k2, zero), preferred_element_type=F32)
                    dcr_sc[h:h + 1, :] += -_colsum(dsc)
                dq_ref[qrows, lanes] += dq2 * ATT_SCALE
                dk_sc[:, lanes] += dk2
                dv_sc[:, lanes] += dv2

        pl.when(i > j)(functools.partial(tile, False))
        pl.when(i == j)(functools.partial(tile, True))

        @pl.when(i == nblk - 1)
        def _():
            dk_ref[...] = dk_sc[...]
            dv_ref[...] = dv_sc[...]
            dcr_ref[...] = dcr_sc[...]

    qi, kj = _tiles(nblk, by_query=False)
    qblk = pl.BlockSpec((t, D_HALF), lambda n, qi, kj: (qi[n], 0))
    qcol = pl.BlockSpec((N_HEADS, t, LANES), lambda n, qi, kj: (0, qi[n], 0))
    kblk = pl.BlockSpec((t, D_HALF), lambda n, qi, kj: (kj[n], 0))
    krow = pl.BlockSpec((N_HEADS, t), lambda n, qi, kj: (0, kj[n]))
    return pl.pallas_call(
        body, name="fox_attn_bwd",
        grid_spec=pltpu.PrefetchScalarGridSpec(
            num_scalar_prefetch=2, grid=(qi.shape[0],),
            in_specs=[qblk, kblk, kblk, qblk, qcol, qcol, qcol, krow],
            out_specs=[pl.BlockSpec((s, D_HALF), lambda n, qi, kj: (0, 0)), kblk, kblk, krow],
            scratch_shapes=[pltpu.VMEM((t, D_HALF), F32), pltpu.VMEM((t, D_HALF), F32), pltpu.VMEM((N_HEADS, t), F32)]),
        out_shape=[jax.ShapeDtypeStruct((s, D_HALF), F32)] * 3 + [jax.ShapeDtypeStruct((N_HEADS, s), F32)],
        compiler_params=_params("arbitrary"),
    )(qi, kj, q, k, v, do, lse, dd, cc, cr)


def _fox_prep_bwd(u_b, h_t, dq, dk, dv, dgate, dcum, fb, qg, kg, tm=256):
    s = u_b.shape[0]
    nb = s // tm
    d = h_t.shape[0]

    def body(ub_ref, ht_ref, dq_ref, dk_ref, dv_ref, dg_ref, dc_ref, fb_ref, qg_ref, kg_ref,
             du_ref, dwb_ref, dqg_ref, dkg_ref, dfb_ref, carry):
        i = pl.program_id(0)

        @pl.when(i == 0)
        def _():
            carry[...] = jnp.zeros_like(carry)
            dwb_ref[...] = jnp.zeros_like(dwb_ref)
            dqg_ref[...] = jnp.zeros_like(dqg_ref)
            dkg_ref[...] = jnp.zeros_like(dkg_ref)
            dfb_ref[...] = jnp.zeros_like(dfb_ref)

        bd = _head_ones()
        for lo, g_ref, d_ref, dgain_ref in ((0, qg_ref, dq_ref, dqg_ref), (512, kg_ref, dk_ref, dkg_ref)):
            gain = g_ref[...]
            xh, rinv, _ = _head_rms(ub_ref[:, lo:lo + 512], gain, bd)
            dn = d_ref[...]
            dgain_ref[...] += _colsum(dn * xh)
            dxh = dn * gain
            du_ref[:, lo:lo + 512] = rinv * (dxh - xh * (_head_sum(dxh * xh, bd) * (1.0 / HEAD)))
        du_ref[:, 1024:1536] = dv_ref[...]
        du_ref[:, 1536:2048] = dg_ref[...]
        lane = lax.broadcasted_iota(jnp.int32, (1, LANES), 1)
        dc = dc_ref[...]
        dlogf = _exact_dot(dc, _tri(tm, False), ones_first=True) + carry[...]
        carry[...] += _colsum(dc)
        fl = ub_ref[:, 2048:2176] + fb_ref[...]
        dfl = jnp.where(lane < N_HEADS, dlogf * (1.0 - _sigmoid(fl)), 0.0)
        du_ref[:, 2048:2176] = dfl
        dfb_ref[...] += _colsum(dfl)
        dwb_ref[...] += jnp.dot(ht_ref[...], du_ref[...].astype(BF16), preferred_element_type=F32)

    rev = lambda w: pl.BlockSpec((tm, w), lambda i: (nb - 1 - i, 0))
    vec = lambda w: pl.BlockSpec((1, w), lambda i: (0, 0))
    return pl.pallas_call(
        body, name="fox_prep_bwd", grid=(nb,),
        in_specs=[rev(SEC), pl.BlockSpec((d, tm), lambda i: (0, nb - 1 - i))] + [rev(D_HALF)] * 4
                 + [rev(LANES), vec(LANES), vec(D_HALF), vec(D_HALF)],
        out_specs=[rev(SEC), pl.BlockSpec((d, SEC), lambda i: (0, 0)), vec(D_HALF), vec(D_HALF), vec(LANES)],
        out_shape=[jax.ShapeDtypeStruct((s, SEC), F32), jax.ShapeDtypeStruct((d, SEC), F32),
                   jax.ShapeDtypeStruct((1, D_HALF), F32), jax.ShapeDtypeStruct((1, D_HALF), F32),
                   jax.ShapeDtypeStruct((1, LANES), F32)],
        scratch_shapes=[pltpu.VMEM((1, LANES), F32)],
        compiler_params=_params("arbitrary"),
    )(u_b, h_t, dq, dk, dv, dgate, dcum, fb, qg, kg)


def _merge(y, r, k, v, gate_a, o, u_b, h, x, tgt, w_g, wa, wb, wo, fg, lw, lb, rk, tm=256):
    s, d = x.shape

    def body(y_ref, r_ref, k_ref, v_ref, ga_ref, o_ref, gb_ref, h_ref, x_ref, t_ref, wg_ref, wa_ref, wb_ref, wo_ref,
             fg_ref, lw_ref, lb_ref, rk_ref,
             dx2_ref, dy_ref, drb_ref, dkb_ref, dvb_ref, dga_ref, do_ref, dgb_ref, dug_ref,
             dwa_ref, dwb_ref, dwo_ref, dfg_ref, loss_ref, dlw_ref, dlb_ref, drk_ref):
        i = pl.program_id(0)

        @pl.when(i == 0)
        def _():
            for ref in (dwa_ref, dwb_ref, dwo_ref, dfg_ref, loss_ref, dlw_ref, dlb_ref, drk_ref):
                ref[...] = jnp.zeros_like(ref)

        bd = _head_ones()
        wa_v, wb_v, wo_v, fg_v = wa_ref[...], wb_ref[...], wo_ref[...], fg_ref[...]
        rv, kv, vv, ga, lw_v, rk_v = r_ref[...], k_ref[...], v_ref[...], ga_ref[...], lw_ref[...], rk_ref[...]
        yn, rstd, rkk, sga, pre = _rwkv_post_math(y_ref[...], rv, kv, vv, ga, lw_v, lb_ref[...], rk_v, bd)
        silu_a = ga * sga
        gb, ov = gb_ref[...], o_ref[...]
        sgb = _sigmoid(gb)
        silu_b = gb * sgb
        ma = (pre * silu_a).astype(BF16)
        mb = (ov * silu_b).astype(BF16)
        ya = jnp.dot(ma, wa_v, preferred_element_type=F32)
        yb = jnp.dot(mb, wb_v, preferred_element_type=F32)
        ug = jnp.dot(h_ref[...], wg_ref[...], preferred_element_type=F32)
        sa = _sigmoid(ug[:, 0:d])
        sb = _sigmoid(ug[:, d:2 * d])
        merged = (sa * ya + sb * yb).astype(BF16)
        x2 = x_ref[...] + jnp.dot(merged, wo_v, preferred_element_type=F32)
        r2 = lax.rsqrt(jnp.mean(x2 * x2, axis=-1, keepdims=True) + RMS_EPS)
        x2h = x2 * r2
        err = x2h * fg_v - t_ref[...]
        loss_ref[...] += _colsum(err * err)
        dyo = err * (1.0 / d)
        dfg_ref[...] += _colsum(dyo * x2h)
        dx2h = dyo * fg_v
        dx2 = r2 * (dx2h - x2h * jnp.mean(dx2h * x2h, axis=-1, keepdims=True))
        dx2_ref[...] = dx2
        dx2b = dx2.astype(BF16)
        dmerged = _dot_nt(dx2b, wo_v)
        dwo_ref[...] += _dot_tn(merged, dx2b)
        dya = dmerged * sa
        dyb = dmerged * sb
        dug_ref[:, 0:d] = dya * ya * (1.0 - sa)
        dug_ref[:, d:2 * d] = dyb * yb * (1.0 - sb)
        dyab = dya.astype(BF16)
        dybb = dyb.astype(BF16)
        dwa_ref[...] += _dot_tn(ma, dyab)
        dwb_ref[...] += _dot_tn(mb, dybb)
        dmb = _dot_nt(dybb, wb_v)
        do_ref[...] = (dmb * silu_b).astype(BF16)
        dgb_ref[...] = dmb * ov * (sgb * (1.0 + gb * (1.0 - sgb)))
        dma = _dot_nt(dyab, wa_v)
        dga_ref[...] = dma * pre * (sga * (1.0 + ga * (1.0 - sga)))
        dpre = dma * silu_a
        dlw_ref[...] += _colsum(dpre * yn)
        dlb_ref[...] += _colsum(dpre)
        dyn = dpre * lw_v
        m1 = _head_sum(dyn, bd) * (1.0 / HEAD)
        m2 = _head_sum(dyn * yn, bd) * (1.0 / HEAD)
        dy_ref[...] = rstd * (dyn - m1 - yn * m2)
        dvb_ref[...] = dpre * rkk
        drkk = _head_sum(dpre * vv, bd)
        drb_ref[...] = drkk * kv * rk_v
        dkb_ref[...] = drkk * rv * rk_v
        drk_ref[...] += _colsum(drkk * rv * kv)

    row = lambda w: pl.BlockSpec((tm, w), lambda i: (i, 0))
    full = lambda a: pl.BlockSpec(a.shape, lambda i: (0, 0))
    once = lambda a: pl.BlockSpec(a.shape, lambda i: (0, 0), pipeline_mode=pl.Buffered(1))
    half = jax.ShapeDtypeStruct((s, D_HALF), F32)
    fshape = lambda a: jax.ShapeDtypeStruct(a.shape, F32)
    return pl.pallas_call(
        body, name="merge_fwd_bwd", grid=(s // tm,),
        in_specs=[row(D_HALF)] * 6 + [pl.BlockSpec((tm, D_HALF), lambda i: (i, 3)), row(d), row(d), row(d),
                                      once(w_g), once(wa), once(wb), once(wo), full(fg), full(lw), full(lb), full(rk)],
        out_specs=[row(d)] + [row(D_HALF)] * 7 + [row(GATE_COLS), full(wa), full(wb), full(wo), full(fg), full(fg),
                                                   full(lw), full(lb), full(rk)],
        out_shape=[jax.ShapeDtypeStruct((s, d), F32)] + [half] * 5 + [jax.ShapeDtypeStruct((s, D_HALF), BF16), half,
                                                                    jax.ShapeDtypeStruct((s, GATE_COLS), F32),
                                                                    fshape(wa), fshape(wb), fshape(wo), fshape(fg),
                                                                    fshape(fg), fshape(lw), fshape(lb), fshape(rk)],
        compiler_params=_params("arbitrary"),
    )(y, r, k, v, gate_a, o, u_b, h, x, tgt, w_g, wa, wb, wo, fg, lw, lb, rk)


def _lora_weight(w_up, a_up):
    z = jnp.zeros((LORA, D_HALF), w_up.dtype)
    return jnp.concatenate([jnp.concatenate([w_up, z], axis=1), jnp.concatenate([z, a_up], axis=1)], axis=0)


def _device_grads(x, tgt, p, w_a, w_up, a_up, late_weights, fwd_exchange=None, bwd_exchange=None, tail_exchange=None):
    wl = _lora_weight(w_up, a_up)
    rk = p["r_k"].reshape(1, D_HALF)
    fb = jnp.pad(p["f_bias"], ((0, 0), (0, LANES - N_HEADS)))
    qg = jnp.tile(p["q_norm_g"], (1, N_HEADS))
    kg = jnp.tile(p["k_norm_g"], (1, N_HEADS))
    fg = p["final_norm_g"].reshape(1, D_MODEL)
    mixer = (p["shift_mu"], wl, p["w0"], p["a0"], p["k_k"], p["k_a"])

    h, u_a, r, dec, k, v, av, bv, gate_a = _rwkv_front(x, p["norm_g"], w_a, *mixer)
    y, st, arrived = _wkv_fwd(r, dec, k, av, bv, v, fwd_exchange)

    w_b, w_g, w_out_a, w_out_b, w_out = late_weights(arrived)
    u_b, q, kn, vb, cc, cr = _fox_front(h, w_b, fb, qg, kg)
    o, lse = _attn_fwd(q, kn, vb, cc, cr)

    (dx2, dy, dr_b, dk_b, dv_b, dgate_a, do, dgate_b, du_g, dwa, dwb, dwo, dfg, loss_vec, dlw, dlb, drk) = _merge(
        y, r, k, v, gate_a, o, u_b, h, x, tgt, w_g, w_out_a, w_out_b, w_out, fg, p["lnx_w"], p["lnx_b"], rk)

    dd = _attn_bwd_rowdot(q, kn, vb, do, lse, cc, cr)
    dq, dk_att, dv_att, dcr = _attn_bwd(q, kn, vb, do, lse, dd, cc, cr)
    dcum = jnp.pad(dcr.T, ((0, 0), (0, LANES - N_HEADS)))
    h_t = h.T
    du_b, dw_b, dqg, dkg, dfb = _fox_prep_bwd(u_b, h_t, dq, dk_att, dv_att, dgate_b, dcum, fb, qg, kg)
    dw_g = _matmul_tn_acc(h_t, du_g, "dw_gate")

    scan_grads, sent = _wkv_bwd(r, dec, k, av, bv, v, dy, st,
                                bwd_exchange(dw_b, dw_g, dwa, dwb, dwo) if bwd_exchange else None)
    du_a, dw_a, dmu, dwl, dw0, da0, dkkw, dkaw = _rwkv_prep_bwd(
        u_a, h_t, (*scan_grads, dr_b, dk_b, dv_b, dgate_a), *mixer)
    dw_up, da_up = dwl[:LORA, :D_HALF], dwl[LORA:, D_HALF:]
    sent_last = _run_on_sequencer(tail_exchange(dw_a, dw_up, da_up), "scatter_tail", 1) if tail_exchange else []
    grad_x, dnorm_g, _ = _inproj_bwd(du_a, du_b, du_g, w_a, w_b, w_g, x, dx2, p["norm_g"])

    grads = dict(
        norm_g=dnorm_g, w_in=(dw_a, dw_b, dw_g), shift_mu=dmu,
        w_lora_up=dw_up, w0=dw0, a_lora_up=da_up, a0=da0, k_k=dkkw, k_a=dkaw,
        r_k=drk.reshape(1, N_HEADS, HEAD), lnx_w=dlw, lnx_b=dlb, f_bias=dfb[:, :N_HEADS],
        q_norm_g=dqg.reshape(N_HEADS, HEAD).sum(axis=0, keepdims=True),
        k_norm_g=dkg.reshape(N_HEADS, HEAD).sum(axis=0, keepdims=True),
        w_out_a=dwa, w_out_b=dwb, w_out=dwo, final_norm_g=dfg.reshape(D_MODEL))
    return loss_vec, grad_x, grads, sent, sent_last


CHIP_FLIPS = ((1, 0), (0, 1), (1, 1))
ANY = pl.BlockSpec(memory_space=pl.ANY)


def _position():
    return lax.axis_index("x"), lax.axis_index("y"), lax.axis_index("c")


def _flip(v, f):
    return 1 - v if f else v


def _both(a, b):
    if a is None:
        return b
    return a if b is None else jnp.logical_and(a, b)


def _when(cond, fn):
    if cond is None:
        fn()
    else:
        pl.when(cond)(fn)


class _Moves:
    def __init__(self, send_sems, recv_sems, local_sems):
        self.send_sems, self.recv_sems, self.local_sems = send_sems, recv_sems, local_sems
        self.remote, self.local = [], []

    def send(self, src, dst, peer, landing, send_if=None, recv_if=None):
        k = len(self.remote)
        sems = dict(send_sem=self.send_sems.at[k], recv_sem=self.recv_sems.at[k], device_id=peer, device_id_type=MESH)
        out = pltpu.make_async_remote_copy(src_ref=src, dst_ref=dst, **sems)
        arrival = pltpu.make_async_remote_copy(src_ref=src, dst_ref=landing, **sems)
        self.remote.append((out, arrival, send_if, recv_if))

    def copy(self, src, dst, cond=None):
        cp = pltpu.make_async_copy(src, dst, self.local_sems.at[len(self.local)])
        self.local.append((cp, cond))

    def start(self, also=None):
        for cp, cond in self.local:
            _when(_both(also, cond), cp.start)
        for out, _, send_if, _ in self.remote:
            _when(_both(also, send_if), out.start)

    def wait_arrivals(self, also=None):
        for _, arrival, _, recv_if in self.remote:
            _when(_both(also, recv_if), arrival.wait_recv)

    def wait_sent(self, also=None):
        for out, _, send_if, _ in self.remote:
            _when(_both(also, send_if), out.wait_send)
        for cp, cond in self.local:
            _when(_both(also, cond), cp.wait)

    def wait(self, also=None):
        self.wait_arrivals(also)
        self.wait_sent(also)


class _Exchange:
    def __init__(self, operands, out_shapes, n_remote, n_local, build, n_relay=0, relay=None):
        self.operands, self.out_shapes = list(operands), list(out_shapes)
        self.n_remote, self.n_local, self.build = n_remote, n_local, build
        self.n_relay, self.relay = n_relay, relay

    def scratch(self):
        return [pltpu.SemaphoreType.DMA((self.n_remote,)), pltpu.SemaphoreType.DMA((self.n_remote,)),
                pltpu.SemaphoreType.DMA((max(self.n_local, 1),))]

    def moves(self, in_refs, out_refs, sems):
        mv = _Moves(*sems)
        self.build(mv, in_refs, out_refs)
        return mv

    def run_alone(self, name):
        n_in, n_out = len(self.operands), len(self.out_shapes)
        relay_scratch = [pltpu.SemaphoreType.DMA((self.n_relay,))] * 2 if self.relay else []

        def body(*refs):
            ins, outs, sems = refs[:n_in], refs[n_in:n_in + n_out], refs[n_in + n_out:]
            mv = self.moves(ins, outs, sems[:3])
            mv.start()
            mv.wait_arrivals()
            if self.relay:
                passed = _Moves(sems[3], sems[4], None)
                self.relay(passed, ins, outs)
                passed.start()
                passed.wait()
            mv.wait_sent()

        return pl.pallas_call(
            body, name=name, in_specs=[ANY] * n_in, out_specs=[ANY] * n_out, out_shape=self.out_shapes,
            scratch_shapes=self.scratch() + relay_scratch, compiler_params=pltpu.CompilerParams(has_side_effects=True),
        )(*self.operands)


def _run_on_sequencer(exchange, name, collective_id):
    ins = [jax.new_ref(a, memory_space=pltpu.MemorySpace.HBM) for a in exchange.operands]
    outs = [jax.empty_ref(s, memory_space=pltpu.MemorySpace.HBM) for s in exchange.out_shapes]
    relay_scratch = [pltpu.SemaphoreType.DMA((exchange.n_relay,))] * 2 if exchange.relay else []

    def launch(*sems):
        x, y, c = _position()
        peers = [(_flip(x, fx), _flip(y, fy), c) for fx, fy in CHIP_FLIPS] + ([(x, y, 1 - c)] if exchange.relay else [])
        barrier = pltpu.get_barrier_semaphore()
        for peer in peers:
            pl.semaphore_signal(barrier, inc=1, device_id=peer, device_id_type=MESH)
        pl.semaphore_wait(barrier, len(peers))
        moves = exchange.moves(ins, outs, sems[:3])
        moves.start()
        moves.wait_arrivals()
        if exchange.relay:
            passed = _Moves(sems[3], sems[4], None)
            exchange.relay(passed, ins, outs)
            passed.start()
            passed.wait()
        moves.wait_sent()

    pl.kernel(launch, mesh=plsc.ScalarSubcoreMesh(axis_name="sequencer", num_cores=1), name=name,
              scratch_types=tuple(exchange.scratch() + relay_scratch),
              compiler_params=pltpu.CompilerParams(collective_id=collective_id))()
    return [o[...] for o in outs]


def _row_major_copy(a, name):
    r, c = a.shape
    tr = _row_tile(r)

    def body(a_ref, o_ref):
        o_ref[...] = a_ref[...]

    blk = pl.BlockSpec((tr, c), lambda i: (i, 0))
    return pl.pallas_call(body, name=name, grid=(r // tr,), in_specs=[blk], out_specs=blk,
                          out_shape=jax.ShapeDtypeStruct(a.shape, a.dtype), compiler_params=_params("parallel"))(a)


def _is_chip(x, y, chip):
    return jnp.logical_and(x == chip // 2, y == chip % 2)


def _gather_exchange(from_chip, from_all, split=()):
    n1, n2 = len(from_chip), len(from_all)

    def rows_of(t, c):
        half = from_chip[t][1].shape[0] // 2
        return pl.ds(c * half, half)

    def build(mv, ins, outs):
        x, y, c = _position()
        me = 2 * x + y
        for t, (chip, _) in enumerate(from_chip):
            mv.copy(ins[t], outs[t], cond=_is_chip(x, y, chip))
        for t in range(n2):
            mv.copy(ins[n1 + t], outs[n1 + t].at[me])
        for fx, fy in CHIP_FLIPS:
            px, py = _flip(x, fx), _flip(y, fy)
            peer = (px, py, c)
            for t, (chip, _) in enumerate(from_chip):
                part = rows_of(t, c) if t in split else slice(None)
                mv.send(ins[t].at[part], outs[t].at[part], peer, landing=outs[t].at[part],
                        send_if=_is_chip(x, y, chip), recv_if=_is_chip(px, py, chip))
            for t in range(n2):
                mv.send(ins[n1 + t], outs[n1 + t].at[me], peer, landing=outs[n1 + t].at[2 * px + py])

    def relay(mv, ins, outs):
        x, y, c = _position()
        for t in split:
            came = jnp.logical_not(_is_chip(x, y, from_chip[t][0]))
            mv.send(outs[t].at[rows_of(t, c)], outs[t].at[rows_of(t, c)], (x, y, 1 - c),
                    landing=outs[t].at[rows_of(t, 1 - c)], send_if=came, recv_if=came)

    arrays = [a for _, a in from_chip] + list(from_all)
    shapes = [jax.ShapeDtypeStruct(a.shape, a.dtype) for _, a in from_chip]
    shapes += [jax.ShapeDtypeStruct((N_CHIPS,) + a.shape, a.dtype) for a in from_all]
    return _Exchange(arrays, shapes, len(CHIP_FLIPS) * (n1 + n2), n1 + n2, build,
                     n_relay=len(split), relay=relay if split else None)


def _scatter_exchange(to_chip, to_all):
    n1, n2 = len(to_chip), len(to_all)

    def build(mv, ins, outs):
        x, y, c = _position()
        for f, (fx, fy) in enumerate(CHIP_FLIPS):
            px, py = _flip(x, fx), _flip(y, fy)
            peer = (px, py, c)
            for t, (chip, _) in enumerate(to_chip):
                mv.send(ins[t], outs[t].at[f], peer, landing=outs[t].at[f],
                        send_if=_is_chip(px, py, chip), recv_if=_is_chip(x, y, chip))
            for t in range(n2):
                mv.send(ins[n1 + t].at[2 * px + py], outs[n1 + t].at[f], peer, landing=outs[n1 + t].at[f])

    arrays = [a for _, a in to_chip] + list(to_all)
    shapes = [jax.ShapeDtypeStruct((len(CHIP_FLIPS),) + a.shape, a.dtype) for _, a in to_chip]
    shapes += [jax.ShapeDtypeStruct((len(CHIP_FLIPS),) + a.shape[1:], a.dtype) for a in to_all]
    return _Exchange(arrays, shapes, len(CHIP_FLIPS) * (n1 + n2), 0, build)


def _swap_sibling(tensors, name):
    n = len(tensors)

    def body(*refs):
        ins, outs = refs[:n], refs[n:2 * n]
        send_sems, recv_sems = refs[2 * n:]
        x, y, c = _position()
        copies = [pltpu.make_async_remote_copy(
            src_ref=ins[t], dst_ref=outs[t], send_sem=send_sems.at[t], recv_sem=recv_sems.at[t],
            device_id=(x, y, 1 - c), device_id_type=MESH) for t in range(n)]
        for cp in copies:
            cp.start()
        for cp in copies:
            cp.wait_recv()
        for cp in copies:
            cp.wait_send()

    return pl.pallas_call(
        body, name=name, in_specs=[ANY] * n, out_specs=[ANY] * n,
        out_shape=[jax.ShapeDtypeStruct(a.shape, a.dtype) for a in tensors],
        scratch_shapes=[pltpu.SemaphoreType.DMA((n,)), pltpu.SemaphoreType.DMA((n,))],
        compiler_params=pltpu.CompilerParams(has_side_effects=True),
    )(*tensors)


def _allreduce_small(slab):
    stages = 3

    def body(x_ref, o_ref, buf, send_sems, recv_sems):
        x, y, c = _position()
        peers = ((1 - x, y, c), (x, 1 - y, c), (x, y, 1 - c))
        o_ref[...] = x_ref[...]
        for k, peer in enumerate(peers):
            cp = pltpu.make_async_remote_copy(src_ref=o_ref, dst_ref=buf.at[k], send_sem=send_sems.at[k],
                                              recv_sem=recv_sems.at[k], device_id=peer, device_id_type=MESH)
            cp.start()
            cp.wait()
            o_ref[...] = o_ref[...] + buf[k]

    return pl.pallas_call(
        body, name="allreduce_small",
        in_specs=[pl.BlockSpec(memory_space=pltpu.VMEM)], out_specs=pl.BlockSpec(memory_space=pltpu.VMEM),
        out_shape=jax.ShapeDtypeStruct(slab.shape, slab.dtype),
        scratch_shapes=[pltpu.VMEM((stages,) + slab.shape, slab.dtype),
                        pltpu.SemaphoreType.DMA((stages,)), pltpu.SemaphoreType.DMA((stages,))],
        compiler_params=pltpu.CompilerParams(has_side_effects=True),
    )(slab)


def _row_tile(r):
    return min(r, 256)


def _sum4(stack, recv, me):
    _, r, c = stack.shape
    tr = _row_tile(r)

    def body(me_ref, own_ref, recv_ref, o_ref):
        o_ref[...] = (((own_ref[...] + recv_ref[0].astype(F32)) + recv_ref[1].astype(F32))
                      + recv_ref[2].astype(F32))

    return pl.pallas_call(
        body, name="sum_partials",
        grid_spec=pltpu.PrefetchScalarGridSpec(
            num_scalar_prefetch=1, grid=(r // tr,),
            in_specs=[pl.BlockSpec((None, tr, c), lambda i, me_ref: (me_ref[0], i, 0)),
                      pl.BlockSpec((len(CHIP_FLIPS), tr, c), lambda i, me_ref: (0, i, 0))],
            out_specs=pl.BlockSpec((tr, c), lambda i, me_ref: (i, 0))),
        out_shape=jax.ShapeDtypeStruct((r, c), F32), compiler_params=_params("parallel"),
    )(me, stack, recv)


def _sum_block(own, recv):
    r, c = own.shape
    tr = _row_tile(r)

    def body(own_ref, recv_ref, o_ref):
        o_ref[...] = (((own_ref[...] + recv_ref[0].astype(F32)) + recv_ref[1].astype(F32))
                      + recv_ref[2].astype(F32))

    return pl.pallas_call(
        body, name="sum_block", grid=(r // tr,),
        in_specs=[pl.BlockSpec((tr, c), lambda i: (i, 0)), pl.BlockSpec((len(CHIP_FLIPS), tr, c), lambda i: (0, i, 0))],
        out_specs=pl.BlockSpec((tr, c), lambda i: (i, 0)),
        out_shape=jax.ShapeDtypeStruct((r, c), F32), compiler_params=_params("parallel"),
    )(own, recv)


def _adamw_math(w, g, m, v):
    m = ADAM_B1 * m + (1.0 - ADAM_B1) * g
    v = ADAM_B2 * v + (1.0 - ADAM_B2) * (g * g)
    m_hat = m / (1.0 - ADAM_B1 ** ADAM_STEP)
    v_hat = v / (1.0 - ADAM_B2 ** ADAM_STEP)
    delta = -ADAM_LR * (m_hat / (jnp.sqrt(v_hat) + ADAM_EPS) + ADAM_WD * w)
    return delta, m, v


def _adamw(w, m, v, g_parts, name):
    r, c = w.shape
    tr = _row_tile(r)
    n = len(g_parts)

    def body(*refs):
        w_ref, m_ref, v_ref = refs[:3]
        g_refs = refs[3:3 + n]
        g_out, d_out, m_out, v_out = refs[3 + n:]
        g = g_refs[0][...]
        for ref in g_refs[1:]:
            g = g + ref[...]
        g_out[...] = g
        d_out[...], m_out[...], v_out[...] = _adamw_math(w_ref[...], g, m_ref[...], v_ref[...])

    blk = pl.BlockSpec((tr, c), lambda i: (i, 0))
    return pl.pallas_call(
        body, name=name, grid=(r // tr,), in_specs=[blk] * (3 + n), out_specs=[blk] * 4,
        out_shape=[jax.ShapeDtypeStruct((r, c), F32)] * 4, compiler_params=_params("parallel"),
    )(w, m, v, *g_parts)


def _adamw_small(total, w, m, v):
    sizes = [w[n].size for n in SMALL]
    flat = lambda d: [d[n].reshape(1, -1) for n in SMALL]
    k = len(SMALL)

    def body(*refs):
        total_ref, w_refs, m_refs, v_refs = refs[0], refs[1:1 + k], refs[1 + k:1 + 2 * k], refs[1 + 2 * k:1 + 3 * k]
        outs = refs[1 + 3 * k:]
        for i, size in enumerate(sizes):
            g = total_ref[i:i + 1, 0:size]
            outs[i][...] = g
            outs[k + i][...], outs[2 * k + i][...], outs[3 * k + i][...] = _adamw_math(
                w_refs[i][...], g, m_refs[i][...], v_refs[i][...])

    res = pl.pallas_call(
        body, name="adamw_small", out_shape=[jax.ShapeDtypeStruct((1, size), F32) for size in sizes] * 4,
        compiler_params=_params(),
    )(total, *flat(w), *flat(m), *flat(v))
    return [{n: res[j * k + i].reshape(w[n].shape) for i, n in enumerate(SMALL)} for j in range(4)]


SHARDED = ("w_in", "w_lora_up", "a_lora_up", "w_out_a", "w_out_b", "w_out")
ROW_SHARDED = ("w_out",)
SMALL = ("norm_g", "shift_mu", "w0", "a0", "k_k", "k_a", "r_k", "lnx_w", "lnx_b", "f_bias", "q_norm_g", "k_norm_g",
         "final_norm_g")
WEIGHTS = ("norm_g", "w_in", "shift_mu", "w_lora_up", "w0", "a_lora_up", "a0", "k_k", "k_a", "r_k", "lnx_w", "lnx_b",
           "f_bias", "q_norm_g", "k_norm_g", "w_out_a", "w_out_b", "w_out", "final_norm_g")
SLAB_ROWS = 16
SLAB_COLS = SEC


def _to_slab(named, extra=None):
    rows = [jnp.pad(named[n].reshape(1, -1), ((0, 0), (0, SLAB_COLS - named[n].size))) for n in SMALL]
    if extra is not None:
        rows.append(jnp.pad(extra.reshape(1, -1), ((0, 0), (0, SLAB_COLS - extra.size))))
    rows.append(jnp.zeros((SLAB_ROWS - len(rows), SLAB_COLS), F32))
    return jnp.concatenate(rows, axis=0)


def _by_chip(g, name):
    if name in ROW_SHARDED:
        return g.reshape(N_CHIPS, g.shape[0] // N_CHIPS, g.shape[1])
    r, c = g.shape
    return g.reshape(r, N_CHIPS, c // N_CHIPS).transpose(1, 0, 2)


def _from_chips(stack, name):
    if name in ROW_SHARDED:
        return stack.reshape(-1, stack.shape[2])
    _, r, c = stack.shape
    return stack.transpose(1, 0, 2).reshape(r, N_CHIPS * c)


def kernel(x, norm_g, w_in, shift_mu, w_lora_up, w0, a_lora_up, a0, k_k, k_a, r_k, lnx_w, lnx_b, f_bias, q_norm_g, k_norm_g, w_out_a, w_out_b, w_out, final_norm_g, loss_target, m_norm_g, m_w_in, m_shift_mu, m_w_lora_up, m_w0, m_a_lora_up, m_a0, m_k_k, m_k_a, m_r_k, m_lnx_w, m_lnx_b, m_f_bias, m_q_norm_g, m_k_norm_g, m_w_out_a, m_w_out_b, m_w_out, m_final_norm_g, v_norm_g, v_w_in, v_shift_mu, v_w_lora_up, v_w0, v_a_lora_up, v_a0, v_k_k, v_k_a, v_r_k, v_lnx_w, v_lnx_b, v_f_bias, v_q_norm_g, v_k_norm_g, v_w_out_a, v_w_out_b, v_w_out, v_final_norm_g):
    w = dict(norm_g=norm_g, w_in=w_in, shift_mu=shift_mu, w_lora_up=w_lora_up, w0=w0, a_lora_up=a_lora_up, a0=a0,
             k_k=k_k, k_a=k_a, r_k=r_k, lnx_w=lnx_w, lnx_b=lnx_b, f_bias=f_bias, q_norm_g=q_norm_g,
             k_norm_g=k_norm_g, w_out_a=w_out_a, w_out_b=w_out_b, w_out=w_out, final_norm_g=final_norm_g)
    m = dict(norm_g=m_norm_g, w_in=m_w_in, shift_mu=m_shift_mu, w_lora_up=m_w_lora_up, w0=m_w0,
             a_lora_up=m_a_lora_up, a0=m_a0, k_k=m_k_k, k_a=m_k_a, r_k=m_r_k, lnx_w=m_lnx_w, lnx_b=m_lnx_b,
             f_bias=m_f_bias, q_norm_g=m_q_norm_g, k_norm_g=m_k_norm_g, w_out_a=m_w_out_a, w_out_b=m_w_out_b,
             w_out=m_w_out, final_norm_g=m_final_norm_g)
    v = dict(norm_g=v_norm_g, w_in=v_w_in, shift_mu=v_shift_mu, w_lora_up=v_w_lora_up, w0=v_w0,
             a_lora_up=v_a_lora_up, a0=v_a0, k_k=v_k_k, k_a=v_k_a, r_k=v_r_k, lnx_w=v_lnx_w, lnx_b=v_lnx_b,
             f_bias=v_f_bias, q_norm_g=v_q_norm_g, k_norm_g=v_k_norm_g, w_out_a=v_w_out_a, w_out_b=v_w_out_b,
             w_out=v_w_out, final_norm_g=v_final_norm_g)
    shapes = {n: w[n].shape for n in WEIGHTS}

    shard = {n: w[n][0].astype(BF16) for n in SHARDED}
    late = ("w_out_a", "w_out_b", "w_out")
    loras = ("w_lora_up", "a_lora_up")
    w_in_head, w_in_tail = shard["w_in"][:, :A_TAIL], shard["w_in"][:, A_TAIL:]
    shard0, shard1_head, up_stack, aup_stack = _run_on_sequencer(_gather_exchange(
        [(0, shard["w_in"]), (1, w_in_head)], [shard[n] for n in loras], split=(0,)), "gather_early", 2)
    moments = (_row_major_copy(m["w_in"][0], "m_w_in_rows"), _row_major_copy(v["w_in"][0], "v_w_in_rows"))
    shard0, moments = lax.optimization_barrier((shard0, moments))
    w_a = jnp.concatenate([shard0, shard1_head], axis=1)

    def late_weights(arrived):
        shard1_tail, shard2, shard3 = arrived[:3]
        w_b = jnp.concatenate([shard1_tail, shard2[:, :B_TAIL], jnp.zeros((D_MODEL, SEC - FOX_REAL), BF16)], axis=1)
        w_g = jnp.concatenate([shard2[:, B_TAIL:], shard3], axis=1)
        return (w_b, w_g, *[_from_chips(s, n) for n, s in zip(late, arrived[3:])])

    own = {}

    def bwd_exchange(dw_b, dw_g, dwa, dwb, dwo):
        own["tail1"] = dw_b[:, :B_HEAD]
        own["block2"] = jnp.concatenate([dw_b[:, B_HEAD:FOX_REAL], dw_g[:, :G_HEAD]], axis=1)
        own["block3"] = dw_g[:, G_HEAD:]
        own.update({n: _by_chip(g, n) for n, g in zip(late, (dwa, dwb, dwo))})
        return _scatter_exchange([(1, own["tail1"].astype(BF16)), (2, own["block2"].astype(BF16)),
                                  (3, own["block3"].astype(BF16))], [own[n].astype(BF16) for n in late])

    def tail_exchange(dw_a, dw_up, da_up):
        own["block0"], own["head1"] = dw_a[:, :SHARD_COLS], dw_a[:, SHARD_COLS:]
        own.update({n: _by_chip(g, n) for n, g in zip(loras, (dw_up, da_up))})
        return _scatter_exchange([(0, own["block0"].astype(BF16)), (1, own["head1"].astype(BF16))],
                                 [own[n].astype(BF16) for n in loras])

    small = {n: w[n] for n in SMALL}
    loss_vec, grad_x, grads, sent, sent_last = _device_grads(
        x[0], loss_target[0], small, w_a, _from_chips(up_stack, "w_lora_up"), _from_chips(aup_stack, "a_lora_up"),
        late_weights, _gather_exchange([(1, w_in_tail), (2, shard["w_in"]), (3, shard["w_in"])], [shard[n] for n in late]),
        bwd_exchange, tail_exchange)

    total = _allreduce_small(_to_slab(grads, extra=loss_vec))
    loss = (0.5 / D_MODEL) * jnp.sum(total[len(SMALL)])
    out_g, out_d, out_m, out_v = _adamw_small(total, w, m, v)

    xpos, ypos, _ = _position()
    me = (2 * xpos + ypos).astype(jnp.int32).reshape(1)
    core_sum = {n: _sum4(own[n], r, me) for n, r in zip(late, sent[3:])}
    theirs = dict(zip(late, _swap_sibling([core_sum[n] for n in late], "swap_sibling_early")))
    sent_last, out_d["norm_g"], theirs = lax.optimization_barrier((sent_last, out_d["norm_g"], theirs))
    core_sum["w_in"] = lax.switch(me[0], [
        lambda: _sum_block(own["block0"], sent_last[0]),
        lambda: jnp.concatenate([_sum_block(own["head1"], sent_last[1]), _sum_block(own["tail1"], sent[0])], axis=1),
        lambda: _sum_block(own["block2"], sent[1]),
        lambda: _sum_block(own["block3"], sent[2])])
    core_sum.update({n: _sum4(own[n], r, me) for n, r in zip(loras, sent_last[2:])})
    rest = ("w_in",) + loras
    theirs.update(zip(rest, _swap_sibling([core_sum[n] for n in rest], "swap_sibling")))
    for n in SHARDED:
        m_n, v_n = moments if n == "w_in" else (m[n][0], v[n][0])
        g, d, m2, v2 = _adamw(w[n][0], m_n, v_n, [core_sum[n], theirs[n]], "adamw_" + n)
        out_g[n], out_d[n], out_m[n], out_v[n] = (a.reshape(shapes[n]) for a in (g, d, m2, v2))

    return (loss, grad_x.reshape(x.shape), *[out_g[n] for n in WEIGHTS], *[out_d[n] for n in WEIGHTS],
            *[out_m[n] for n in WEIGHTS], *[out_v[n] for n in WEIGHTS])
```

```python
import functools
import math

import jax
import jax.numpy as jnp
from jax import lax
from jax.experimental import pallas as pl
from jax.experimental.pallas import tpu as pltpu
from jax.experimental.pallas import tpu_sc as plsc

F32 = jnp.float32
BF16 = jnp.bfloat16

D_MODEL = 1024
D_HALF = 512
HEAD = 64
N_HEADS = 8
LORA = 64
RWKV_COLS = 2176
FOX_REAL = 2056
SEC = 2176
GATE_COLS = 2048
IN_COLS = 6280
N_CHIPS = 4
SHARD_COLS = IN_COLS // N_CHIPS
A_TAIL = RWKV_COLS - SHARD_COLS
B_HEAD = SHARD_COLS - A_TAIL
B_TAIL = FOX_REAL - B_HEAD
G_HEAD = SHARD_COLS - B_TAIL
RMS_EPS = 1e-6
LNX_EPS = 64e-5
ATT_SCALE = HEAD ** -0.5
NEG = -1e30

ADAM_LR = 0.001
ADAM_B1 = 0.9
ADAM_B2 = 0.999
ADAM_EPS = 1e-08
ADAM_WD = 0.01
ADAM_STEP = 10

LANES = 128
SUBLANES = 8
VMEM_LIMIT = 56 * 1024 * 1024
MESH = pl.DeviceIdType.MESH


def _params(*sem):
    return pltpu.CompilerParams(dimension_semantics=sem if sem else None, vmem_limit_bytes=VMEM_LIMIT)


def _sigmoid(x):
    return 1.0 / (1.0 + jnp.exp(-x))


def _log_sigmoid(x):
    return jnp.minimum(x, 0.0) - jnp.log(1.0 + jnp.exp(-jnp.abs(x)))


def _head_ones():
    r = lax.broadcasted_iota(jnp.int32, (LANES, LANES), 0) >> 6
    c = lax.broadcasted_iota(jnp.int32, (LANES, LANES), 1) >> 6
    return (r == c).astype(BF16)


def _split3(x):
    hi = x.astype(BF16)
    r1 = x - hi.astype(F32)
    mid = r1.astype(BF16)
    lo = (r1 - mid.astype(F32)).astype(BF16)
    return hi, mid, lo


def _exact_dot(x, ones_bf16, ones_first=False):
    out = None
    for piece in _split3(x):
        if ones_first:
            t = jnp.dot(ones_bf16, piece, preferred_element_type=F32)
        else:
            t = jnp.dot(piece, ones_bf16, preferred_element_type=F32)
        out = t if out is None else out + t
    return out


def _head_sum(x, bd):
    n = x.shape[1] // LANES
    parts = [_exact_dot(x[:, i * LANES:(i + 1) * LANES], bd) for i in range(n)]
    return parts[0] if n == 1 else jnp.concatenate(parts, axis=1)


def _dot_nt(a, b):
    return lax.dot_general(a, b, (((1,), (1,)), ((), ())), preferred_element_type=F32)


def _dot_tn(a, b):
    return lax.dot_general(a, b, (((0,), (0,)), ((), ())), preferred_element_type=F32)


def _colsum(x):
    return jnp.sum(x, axis=0, keepdims=True)


def _matmul_tn_acc(at, b, name, tk=512):
    m, k = at.shape
    n = b.shape[1]

    def body(a_ref, b_ref, o_ref):
        j = pl.program_id(0)

        @pl.when(j == 0)
        def _():
            o_ref[...] = jnp.zeros_like(o_ref)

        o_ref[...] += jnp.dot(a_ref[...], b_ref[...].astype(BF16), preferred_element_type=F32)

    return pl.pallas_call(
        body, name=name, grid=(k // tk,),
        in_specs=[pl.BlockSpec((m, tk), lambda j: (0, j)), pl.BlockSpec((tk, n), lambda j: (j, 0))],
        out_specs=pl.BlockSpec((m, n), lambda j: (0, 0)),
        out_shape=jax.ShapeDtypeStruct((m, n), F32), compiler_params=_params("arbitrary"),
    )(at, b)


def _inproj_bwd(du_a, du_b, du_g, w_a, w_b, w_g, x, dx2, g, exchange=None, tm=256):
    s, d = x.shape
    nb = s // tm

    def body(*refs):
        ((da_ref, db_ref, dg_ref, wa_ref, wb_ref, wg_ref, x_ref, dx2_ref, g_ref), (gx_ref, gg_ref), _,
         moves) = _split_refs(refs, 9, 2, exchange)
        i = pl.program_id(0)
        if moves:
            moves.start(also=(i == 0))

        @pl.when(i == 0)
        def _():
            gg_ref[...] = jnp.zeros_like(gg_ref)

        dh = _dot_nt(da_ref[...].astype(BF16), wa_ref[...])
        dh += _dot_nt(db_ref[...].astype(BF16), wb_ref[...])
        dh += _dot_nt(dg_ref[...].astype(BF16), wg_ref[...])
        xv = x_ref[...]
        r = lax.rsqrt(jnp.mean(xv * xv, axis=-1, keepdims=True) + RMS_EPS)
        xh = xv * r
        gg_ref[...] += _colsum(dh * xh)
        dxh = dh * g_ref[...]
        gx_ref[...] = dx2_ref[...] + r * (dxh - xh * jnp.mean(dxh * xh, axis=-1, keepdims=True))
        if moves:
            moves.wait(also=(i == nb - 1))

    row = lambda w: pl.BlockSpec((tm, w), lambda i: (i, 0))
    full = lambda a: pl.BlockSpec(a.shape, lambda i: (0, 0))
    ex_in = exchange.operands if exchange else []
    ex_out = exchange.out_shapes if exchange else []
    res = pl.pallas_call(
        body, name="inproj_bwd", grid=(nb,),
        in_specs=[row(SEC), row(SEC), row(GATE_COLS), full(w_a), full(w_b), full(w_g), row(d), row(d), full(g)]
                 + [ANY] * len(ex_in),
        out_specs=[row(d), pl.BlockSpec((1, d), lambda i: (0, 0))] + [ANY] * len(ex_out),
        out_shape=[jax.ShapeDtypeStruct((s, d), F32), jax.ShapeDtypeStruct((1, d), F32)] + ex_out,
        scratch_shapes=exchange.scratch() if exchange else [],
        compiler_params=_params("arbitrary"),
    )(du_a, du_b, du_g, w_a, w_b, w_g, x, dx2, g, *ex_in)
    return res[0], res[1], list(res[2:])


def _rwkv_elementwise(ua, prev_row, first, mu, wl, w0, a0, kkw, kaw, bd):
    tm = ua.shape[0]
    rows = lax.broadcasted_iota(jnp.int32, (tm, 1), 0)
    prev = jnp.where(first, jnp.zeros_like(prev_row), prev_row)
    shifted = jnp.where(rows == 0, prev, pltpu.roll(ua, 1, 0))
    delta = shifted - ua
    us = ua + delta * mu
    r = us[:, 0:512]
    k0 = us[:, 512:1024]
    v = us[:, 1024:1536]
    lo = us[:, 1536:1664]
    gate = us[:, 1664:2176]
    lane = lax.broadcasted_iota(jnp.int32, (1, LANES), 1)
    th = jnp.tanh(lo)
    lin = jnp.where(lane < LORA, th, lo)
    ll = jnp.dot(lin.astype(BF16), wl, preferred_element_type=F32)
    sz = _sigmoid(w0 + ll[:, :512])
    e = sz * math.exp(-0.5)
    dec = jnp.exp(-e)
    a = _sigmoid(a0 + ll[:, 512:])
    kk0 = k0 * kkw
    ss = _head_sum(kk0 * kk0, bd)
    nrm = jnp.maximum(jnp.sqrt(ss), 1e-12)
    kk = kk0 / nrm
    k = k0 * (1.0 + (a - 1.0) * kaw)
    return dict(delta=delta, us=us, r=r, k0=k0, v=v, lo=lo, gate=gate, th=th, lin=lin, sz=sz, e=e, dec=dec,
                a=a, kk0=kk0, ss=ss, nrm=nrm, kk=kk, k=k)


def _rwkv_front(x, g, w_a, mu, wl, w0, a0, kkw, kaw, tm=256):
    s, d = x.shape

    def body(x_ref, g_ref, wa_ref, mu_ref, wl_ref, w0_ref, a0_ref, kkw_ref, kaw_ref,
             h_ref, ua_ref, r_ref, w_ref, k_ref, v_ref, a_ref, b_ref, gate_ref, last_row):
        i = pl.program_id(0)
        xv = x_ref[...]
        h = (xv * lax.rsqrt(jnp.mean(xv * xv, axis=-1, keepdims=True) + RMS_EPS) * g_ref[...]).astype(BF16)
        h_ref[...] = h
        ua = jnp.dot(h, wa_ref[...], preferred_element_type=F32)
        ua_ref[...] = ua

        @pl.when(i == 0)
        def _():
            last_row[...] = jnp.zeros_like(last_row)

        f = _rwkv_elementwise(ua, last_row[...], i == 0, mu_ref[...], wl_ref[...], w0_ref[...],
                              a0_ref[...], kkw_ref[...], kaw_ref[...], _head_ones())
        last_row[...] = ua[tm - 1:tm, :]
        r_ref[...] = f["r"]
        w_ref[...] = f["dec"]
        k_ref[...] = f["k"]
        v_ref[...] = f["v"]
        a_ref[...] = -f["kk"]
        b_ref[...] = f["kk"] * f["a"]
        gate_ref[...] = f["gate"]

    vec = lambda w: pl.BlockSpec((1, w), lambda i: (0, 0))
    row = lambda w: pl.BlockSpec((tm, w), lambda i: (i, 0))
    return pl.pallas_call(
        body, name="rwkv_front", grid=(s // tm,),
        in_specs=[row(d), vec(d), pl.BlockSpec(w_a.shape, lambda i: (0, 0), pipeline_mode=pl.Buffered(1)),
                  vec(SEC), pl.BlockSpec((LANES, 2 * D_HALF), lambda i: (0, 0)),
                  vec(D_HALF), vec(D_HALF), vec(D_HALF), vec(D_HALF)],
        out_specs=[row(d), row(SEC)] + [row(D_HALF)] * 7,
        out_shape=[jax.ShapeDtypeStruct((s, d), BF16), jax.ShapeDtypeStruct((s, SEC), F32)]
                  + [jax.ShapeDtypeStruct((s, D_HALF), F32)] * 7,
        scratch_shapes=[pltpu.VMEM((1, SEC), F32)],
        compiler_params=_params("arbitrary"),
    )(x, g, w_a, mu, wl, w0, a0, kkw, kaw)


SCAN_TB = 128
N_PAIRS = 4


def _pair_sum(x, left):
    s_l = jnp.sum(jnp.where(left, x, 0.0), axis=1, keepdims=True)
    s_r = jnp.sum(jnp.where(left, 0.0, x), axis=1, keepdims=True)
    return jnp.where(left, s_l, s_r)


def _pair_dot(x, row_l, row_r, left):
    s_l = jnp.sum(x * row_l, axis=1, keepdims=True)
    s_r = jnp.sum(x * row_r, axis=1, keepdims=True)
    return jnp.where(left, s_l, s_r)


def _halves(rows8):
    lane = lax.broadcasted_iota(jnp.int32, rows8.shape, 1)
    keep_left = (lane & (LANES - 1)) < HEAD
    return jnp.where(keep_left, rows8, 0.0), jnp.where(keep_left, 0.0, rows8)


def _quad_consts():
    lane = lax.broadcasted_iota(jnp.int32, (HEAD, 2 * LANES), 1)
    rowi = lax.broadcasted_iota(jnp.int32, (HEAD, 2 * LANES), 0)
    diag2 = rowi == (lane & (HEAD - 1))
    r = lax.broadcasted_iota(jnp.int32, (2 * LANES, 2 * LANES), 0) >> 6
    c = lax.broadcasted_iota(jnp.int32, (2 * LANES, 2 * LANES), 1) >> 6
    return diag2, (r == c).astype(BF16)


def _rows_to_columns(x8, diag2, bd2):
    lhs = jnp.concatenate([jnp.where(diag2, x8[i:i + 1], 0.0).astype(BF16) for i in range(SUBLANES)], axis=0)
    return jnp.dot(lhs, bd2, preferred_element_type=F32)


def _diag_rows(qtile, diag2, bd2, sub_row2):
    res = jnp.dot(qtile, bd2, preferred_element_type=F32)
    out = jnp.zeros((SUBLANES, 2 * LANES), F32)
    for i in range(SUBLANES):
        out = jnp.where(sub_row2 == i, _colsum(jnp.where(diag2, res[i * HEAD:(i + 1) * HEAD], 0.0)), out)
    return out


def _store_tile(qbuf, slot, p, i, x):
    qbuf[slot, p // 2, i * HEAD:(i + 1) * HEAD, (p % 2) * LANES:(p % 2 + 1) * LANES] = x.astype(BF16)


def _left_half():
    return lax.broadcasted_iota(jnp.int32, (HEAD, LANES), 1) < HEAD


def _split_refs(refs, n_rows, n_out, exchange):
    n_in = len(exchange.operands) if exchange else 0
    n_ex_out = len(exchange.out_shapes) if exchange else 0
    refs = list(refs)
    rows, refs = refs[:n_rows], refs[n_rows:]
    ex_in, refs = refs[:n_in], refs[n_in:]
    outs, refs = refs[:n_out], refs[n_out:]
    ex_out, refs = refs[:n_ex_out], refs[n_ex_out:]
    scratch, sems = (refs[:-3], refs[-3:]) if exchange else (refs, None)
    moves = exchange.moves(ex_in, ex_out, sems) if exchange else None
    return rows, outs, scratch, moves


def _wkv_fwd(r, w, k, a, b, v, exchange=None):
    s = r.shape[0]
    tb = SCAN_TB
    nb = s // tb

    def body(*refs):
        (r_ref, w_ref, k_ref, a_ref, b_ref, v_ref), (y_ref, st_ref), (state, vbuf, qbuf), moves = _split_refs(
            refs, 6, 2, exchange)
        g = pl.program_id(0)
        if moves:
            moves.start(also=(g == 0))

        @pl.when(g == 0)
        def _():
            state[...] = jnp.zeros_like(state)
            qbuf[...] = jnp.zeros_like(qbuf)

        left = _left_half()
        diag2, bd2 = _quad_consts()
        sub_row2 = lax.broadcasted_iota(jnp.int32, (SUBLANES, 2 * LANES), 0)
        groups = tb // SUBLANES
        quads = [slice(g2 * 2 * LANES, (g2 + 1) * 2 * LANES) for g2 in range(2)]

        def rows_of(q):
            return pl.ds(pl.multiple_of(q * SUBLANES, SUBLANES), SUBLANES)

        def v_tiles(q, slot):
            v8 = v_ref[rows_of(q), :]
            for g2 in range(2):
                vbuf[slot, g2] = _rows_to_columns(v8[:, quads[g2]], diag2, bd2)

        def chain(q, slot):
            rows8 = rows_of(q)
            a8, w8, b8, k8, r8 = (x[rows8, :] for x in (a_ref, w_ref, b_ref, k_ref, r_ref))
            pairs = [slice(p * LANES, (p + 1) * LANES) for p in range(N_PAIRS)]
            a_next = pltpu.roll(a8, SUBLANES - 1, 0)
            (a8_l, a8_r), (wa8_l, wa8_r) = _halves(a8), _halves(w8 * a_next)
            ba8 =jnp.concatenate([_pair_sum(b8[:, pr] * a_next[:, pr], left[0:SUBLANES]) for pr in pairs], axis=1)
            ka8 = jnp.concatenate([_pair_sum(k8[:, pr] * a_next[:, pr], left[0:SUBLANES]) for pr in pairs], axis=1)
            sp = [state[p] for p in range(N_PAIRS)]
            for i in range(0, SUBLANES, 2):
                r0, r1 = slice(i, i + 1), slice(i + 1, i + 2)
                sums = [(_pair_dot(sp[p], a8_l[r0, pairs[p]], a8_r[r0, pairs[p]], left),
                         _pair_dot(sp[p], wa8_l[r0, pairs[p]], wa8_r[r0, pairs[p]], left)) for p in range(N_PAIRS)]
                sa0, sa1 = [s[0] for s in sums], [s[1] for s in sums]
                for p in range(N_PAIRS):
                    pr = pairs[p]
                    inner = slice((p % 2) * LANES, (p % 2 + 1) * LANES)
                    vt0 = vbuf[slot, p // 2, i * HEAD:(i + 1) * HEAD, inner]
                    vt1 = vbuf[slot, p // 2, (i + 1) * HEAD:(i + 2) * HEAD, inner]
                    sa_next = sa1[p] + sa0[p] * ba8[r0, pr] + vt0 * ka8[r0, pr]
                    s1 = sp[p] * w8[r0, pr] + sa0[p] * b8[r0, pr] + vt0 * k8[r0, pr]
                    st_ref[q * SUBLANES + i, p] = s1
                    _store_tile(qbuf, slot, p, i, s1 * r8[r0, pr])
                    s2 = s1 * w8[r1, pr] + sa_next * b8[r1, pr] + vt1 * k8[r1, pr]
                    st_ref[q * SUBLANES + i + 1, p] = s2
                    _store_tile(qbuf, slot, p, i + 1, s2 * r8[r1, pr])
                    sp[p] = s2
            for p in range(N_PAIRS):
                state[p] = sp[p]

        def y_rows(q, slot):
            for g2 in range(2):
                y_ref[rows_of(q), quads[g2]] = _diag_rows(qbuf[slot, g2], diag2, bd2, sub_row2)

        v_tiles(0, 0)

        def two_groups(j, carry):
            q0 = 2 * j
            v_tiles(q0 + 1, 1)
            chain(q0, 0)
            y_rows(jnp.maximum(q0 - 1, 0), 1)
            v_tiles(jnp.minimum(q0 + 2, groups - 1), 0)
            chain(q0 + 1, 1)
            y_rows(q0, 0)
            return carry

        lax.fori_loop(0, groups // 2, two_groups, 0)
        y_rows(groups - 1, 1)
        if moves:
            moves.wait(also=(g == nb - 1))

    rows = pl.BlockSpec((tb, D_HALF), lambda g: (g, 0))
    ex_in = exchange.operands if exchange else []
    ex_out = exchange.out_shapes if exchange else []
    res = pl.pallas_call(
        body, name="wkv_fwd", grid=(nb,),
        in_specs=[rows] * 6 + [ANY] * len(ex_in),
        out_specs=[rows, pl.BlockSpec((tb, N_PAIRS, HEAD, LANES), lambda g: (g, 0, 0, 0))] + [ANY] * len(ex_out),
        out_shape=[jax.ShapeDtypeStruct((s, D_HALF), F32),
                   jax.ShapeDtypeStruct((s, N_PAIRS, HEAD, LANES), F32)] + ex_out,
        scratch_shapes=[pltpu.VMEM((N_PAIRS, HEAD, LANES), F32),
                        pltpu.VMEM((2, 2, SUBLANES * HEAD, 2 * LANES), F32),
                        pltpu.VMEM((2, 2, SUBLANES * HEAD, 2 * LANES), BF16)]
                       + (exchange.scratch() if exchange else []),
        compiler_params=_params("arbitrary"),
    )(r, w, k, a, b, v, *ex_in)
    return res[0], res[1], list(res[2:])


def _wkv_bwd(r, w, k, a, b, v, dy, st, exchange=None):
    s = r.shape[0]
    tb = SCAN_TB
    nb = s // tb

    def body(*refs):
        ((r_ref, w_ref, k_ref, a_ref, b_ref, v_ref, dy_ref, st_ref, before_ref),
         (dr_ref, dw_ref, dk_ref, dv_ref, da_ref, db_ref), (dstate, vbuf, qbuf, sbuf),
         moves) = _split_refs(refs, 9, 6, exchange)
        g = pl.program_id(0)
        first_block = g == nb - 1
        if moves:
            moves.start(also=(g == 0))

        @pl.when(g == 0)
        def _():
            dstate[...] = jnp.zeros_like(dstate)
            qbuf[...] = jnp.zeros_like(qbuf)

        left = _left_half()
        diag2, bd2 = _quad_consts()
        sub_row = lax.broadcasted_iota(jnp.int32, (SUBLANES, LANES), 0)
        sub_row2 = lax.broadcasted_iota(jnp.int32, (SUBLANES, 2 * LANES), 0)
        groups = tb // SUBLANES
        quads = [slice(g2 * 2 * LANES, (g2 + 1) * 2 * LANES) for g2 in range(2)]
        row_refs = (dr_ref, dw_ref, dk_ref, da_ref, db_ref)

        def rows_of(q):
            return pl.ds(pl.multiple_of(q * SUBLANES, SUBLANES), SUBLANES)

        def state_before(q, i, p):
            if i > 0:
                return st_ref[q * SUBLANES + i - 1, p]
            return jnp.where(q == 0, jnp.where(first_block, 0.0, before_ref[0, p]),
                             st_ref[jnp.maximum(q * SUBLANES - 1, 0), p])

        def column_tiles(q, slot):
            rows8 = rows_of(q)
            for kind, ref in enumerate((v_ref, dy_ref)):
                x8 = ref[rows8, :]
                for g2 in range(2):
                    vbuf[slot, kind, g2] = _rows_to_columns(x8[:, quads[g2]], diag2, bd2)
            a8 = a_ref[rows8, :]
            for i in range(SUBLANES):
                for p in range(N_PAIRS):
                    _store_tile(sbuf, 0, p, i, state_before(q, i, p) * a8[i:i + 1, p * LANES:(p + 1) * LANES])
            for g2 in range(2):
                vbuf[slot, 2, g2] = jnp.dot(sbuf[0, g2], bd2, preferred_element_type=F32)

        def chain(q, slot):
            rows8 = rows_of(q)
            a8, w8, b8, k8, r8 = (x[rows8, :] for x in (a_ref, w_ref, b_ref, k_ref, r_ref))
            b8_l, b8_r = _halves(b8)
            dsp = [dstate[p] for p in range(N_PAIRS)]
            outs = [[jnp.zeros((SUBLANES, LANES), F32) for _ in row_refs] for _ in range(N_PAIRS)]
            after = [st_ref[q * SUBLANES + SUBLANES - 1, p] for p in range(N_PAIRS)]
            for i in reversed(range(SUBLANES)):
                row = slice(i, i + 1)
                pl_ = [slice(p * LANES, (p + 1) * LANES) for p in range(N_PAIRS)]
                tile = [(p // 2, slice(i * HEAD, (i + 1) * HEAD), slice((p % 2) * LANES, (p % 2 + 1) * LANES))
                        for p in range(N_PAIRS)]
                sp = [state_before(q, i, p) for p in range(N_PAIRS)]
                dyt = [vbuf[(slot, 1) + tile[p]] for p in range(N_PAIRS)]
                ds = [dsp[p] + dyt[p] * r8[row, pl_[p]] for p in range(N_PAIRS)]
                dsa = [_pair_dot(ds[p], b8_l[row, pl_[p]], b8_r[row, pl_[p]], left) for p in range(N_PAIRS)]
                sa = [vbuf[(slot, 2) + tile[p]] for p in range(N_PAIRS)]
                for p in range(N_PAIRS):
                    ar, wr, br, kr = (x[row, pl_[p]] for x in (a8, w8, b8, k8))
                    vt = vbuf[(slot, 0) + tile[p]]
                    dsp[p] = ds[p] * wr + dsa[p] * ar
                    new = (_colsum(after[p] * dyt[p]), _colsum(ds[p] * sp[p]), _colsum(ds[p] * vt),
                           _colsum(sp[p] * dsa[p]), _colsum(ds[p] * sa[p]))
                    outs[p] = [jnp.where(sub_row == i, n, o) for n, o in zip(new, outs[p])]
                    _store_tile(qbuf, slot, p, i, ds[p] * kr)
                after = sp
            for p in range(N_PAIRS):
                dstate[p] = dsp[p]
                for ref, o in zip(row_refs, outs[p]):
                    ref[rows8, p * LANES:(p + 1) * LANES] = o

        def dv_rows(q, slot):
            for g2 in range(2):
                dv_ref[rows_of(q), quads[g2]] = _diag_rows(qbuf[slot, g2], diag2, bd2, sub_row2)

        column_tiles(groups - 1, 0)

        def two_groups(j, carry):
            q0 = groups - 1 - 2 * j
            column_tiles(q0 - 1, 1)
            chain(q0, 0)
            dv_rows(jnp.minimum(q0 + 1, groups - 1), 1)
            column_tiles(jnp.maximum(q0 - 2, 0), 0)
            chain(q0 - 1, 1)
            dv_rows(q0, 0)
            return carry

        lax.fori_loop(0, groups // 2, two_groups, 0)
        dv_rows(0, 1)
        if moves:
            moves.wait(also=(g == nb - 1))

    rows = pl.BlockSpec((tb, D_HALF), lambda g: (nb - 1 - g, 0))
    ex_in = exchange.operands if exchange else []
    ex_out = exchange.out_shapes if exchange else []
    res = pl.pallas_call(
        body, name="wkv_bwd", grid=(nb,),
        in_specs=[rows] * 7 + [pl.BlockSpec((tb, N_PAIRS, HEAD, LANES), lambda g: (nb - 1 - g, 0, 0, 0)),
                               pl.BlockSpec((1, N_PAIRS, HEAD, LANES),
                                            lambda g: (jnp.maximum((nb - 1 - g) * tb - 1, 0), 0, 0, 0))]
                 + [ANY] * len(ex_in),
        out_specs=[rows] * 6 + [ANY] * len(ex_out),
        out_shape=[jax.ShapeDtypeStruct((s, D_HALF), F32)] * 6 + ex_out,
        scratch_shapes=[pltpu.VMEM((N_PAIRS, HEAD, LANES), F32),
                        pltpu.VMEM((2, 3, 2, SUBLANES * HEAD, 2 * LANES), F32),
                        pltpu.VMEM((2, 2, SUBLANES * HEAD, 2 * LANES), BF16),
                        pltpu.VMEM((1, 2, SUBLANES * HEAD, 2 * LANES), BF16)]
                       + (exchange.scratch() if exchange else []),
        compiler_params=_params("arbitrary"),
    )(r, w, k, a, b, v, dy, st, st, *ex_in)
    return list(res[:6]), list(res[6:])


def _rwkv_post_math(y, r, k, v, gate, lw, lb, rk, bd):
    mean = _head_sum(y, bd) * (1.0 / HEAD)
    yc = y - mean
    var = _head_sum(yc * yc, bd) * (1.0 / HEAD)
    rstd = lax.rsqrt(var + LNX_EPS)
    yn = yc * rstd
    rkk = _head_sum(r * k * rk, bd)
    sg = _sigmoid(gate)
    pre = yn * lw + lb + rkk * v
    return yn, rstd, rkk, sg, pre


def _rwkv_prep_bwd(u_a, h_t, grads, mu, wl, w0, a0, kkw, kaw, tm=256):
    s = u_a.shape[0]
    nb = s // tm
    d = h_t.shape[0]

    def body(ua_ref, prev_ref, ht_ref, drs_ref, dws_ref, dks_ref, dvs_ref, das_ref, dbs_ref, drb_ref, dkb_ref, dvb_ref,
             dgt_ref, mu_ref, wl_ref, w0_ref, a0_ref, kkw_ref, kaw_ref,
             du_ref, dwa_ref, dmu_ref, dwl_ref, dw0_ref, da0_ref, dkkw_ref, dkaw_ref, carry):
        i = pl.program_id(0)

        @pl.when(i == 0)
        def _():
            carry[...] = jnp.zeros_like(carry)
            for ref in (dwa_ref, dmu_ref, dwl_ref, dw0_ref, da0_ref, dkkw_ref, dkaw_ref):
                ref[...] = jnp.zeros_like(ref)

        bd = _head_ones()
        mu_v, wl_v, kkw_v, kaw_v = mu_ref[...], wl_ref[...], kkw_ref[...], kaw_ref[...]
        f = _rwkv_elementwise(ua_ref[...], prev_ref[7:8, :], i == nb - 1, mu_v, wl_v, w0_ref[...],
                              a0_ref[...], kkw_v, kaw_v, bd)
        a, kk, k0 = f["a"], f["kk"], f["k0"]
        dk = dks_ref[...] + dkb_ref[...]
        dbs = dbs_ref[...]
        dkk = dbs * a - das_ref[...]
        da = dbs * kk + dk * k0 * kaw_v
        dk0 = dk * (1.0 + (a - 1.0) * kaw_v)
        dkaw_ref[...] += _colsum(dk * k0 * (a - 1.0))
        inv = 1.0 / f["nrm"]
        proj = _head_sum(dkk * kk, bd)
        dkk0 = jnp.where(f["ss"] > 1e-24, (dkk - kk * proj) * inv, dkk * inv)
        dk0 = dk0 + dkk0 * kkw_v
        dkkw_ref[...] += _colsum(dkk0 * k0)
        dza = da * a * (1.0 - a)
        da0_ref[...] += _colsum(dza)
        dz = -dws_ref[...] * f["dec"] * f["e"] * (1.0 - f["sz"])
        dw0_ref[...] += _colsum(dz)
        dll = jnp.concatenate([dz, dza], axis=1).astype(BF16)
        dwl_ref[...] += _dot_tn(f["lin"].astype(BF16), dll)
        dlin = _dot_nt(dll, wl_v)
        lane = lax.broadcasted_iota(jnp.int32, (1, LANES), 1)
        th = f["th"]
        dlo = jnp.where(lane < LORA, dlin * (1.0 - th * th), dlin)
        dus = jnp.concatenate([drs_ref[...] + drb_ref[...], dk0, dvs_ref[...] + dvb_ref[...], dlo, dgt_ref[...]],
                              axis=1)
        dmu_ref[...] += _colsum(dus * f["delta"])
        g1 = dus * mu_v
        rows = lax.broadcasted_iota(jnp.int32, (tm, 1), 0)
        up = jnp.where(rows == tm - 1, carry[...], pltpu.roll(g1, tm - 1, 0))
        dua = dus - g1 + up
        du_ref[...] = dua
        dwa_ref[...] += jnp.dot(ht_ref[...], dua.astype(BF16), preferred_element_type=F32)
        carry[...] = g1[0:1, :]

    rev = lambda w: pl.BlockSpec((tm, w), lambda i: (nb - 1 - i, 0))
    vec = lambda w: pl.BlockSpec((1, w), lambda i: (0, 0))
    wl_spec = pl.BlockSpec((LANES, 2 * D_HALF), lambda i: (0, 0))
    return pl.pallas_call(
        body, name="rwkv_prep_bwd", grid=(nb,),
        in_specs=[rev(SEC), pl.BlockSpec((8, SEC), lambda i: (jnp.maximum((nb - 1 - i) * (tm // 8) - 1, 0), 0)),
                  pl.BlockSpec((d, tm), lambda i: (0, nb - 1 - i))]
                 + [rev(D_HALF)] * 10 + [vec(SEC), wl_spec] + [vec(D_HALF)] * 4,
        out_specs=[rev(SEC), pl.BlockSpec((d, SEC), lambda i: (0, 0)), vec(SEC), wl_spec] + [vec(D_HALF)] * 4,
        out_shape=[jax.ShapeDtypeStruct((s, SEC), F32), jax.ShapeDtypeStruct((d, SEC), F32),
                   jax.ShapeDtypeStruct((1, SEC), F32),
                   jax.ShapeDtypeStruct((LANES, 2 * D_HALF), F32)] + [jax.ShapeDtypeStruct((1, D_HALF), F32)] * 4,
        scratch_shapes=[pltpu.VMEM((1, SEC), F32)],
        compiler_params=_params("arbitrary"),
    )(u_a, u_a, h_t, *grads, mu, wl, w0, a0, kkw, kaw)


def _tri(tm, lower):
    r = lax.broadcasted_iota(jnp.int32, (tm, tm), 0)
    c = lax.broadcasted_iota(jnp.int32, (tm, tm), 1)
    return ((r >= c) if lower else (r <= c)).astype(BF16)


def _head_rms(x, g, bd):
    rinv = lax.rsqrt(_head_sum(x * x, bd) * (1.0 / HEAD) + RMS_EPS)
    xh = x * rinv
    return xh, rinv, xh * g


def _fox_front(h, w_b, fb, qg, kg, tm=256):
    s, d = h.shape

    def body(h_ref, wb_ref, fb_ref, qg_ref, kg_ref, ub_ref, q_ref, k_ref, v_ref, cc_ref, cr_ref, carry):
        i = pl.program_id(0)

        @pl.when(i == 0)
        def _():
            carry[...] = jnp.zeros_like(carry)

        ub_ref[...] = jnp.dot(h_ref[...], wb_ref[...], preferred_element_type=F32)
        bd = _head_ones()
        _, _, qn = _head_rms(ub_ref[:, 0:512], qg_ref[...], bd)
        _, _, kn = _head_rms(ub_ref[:, 512:1024], kg_ref[...], bd)
        q_ref[...] = (qn * ATT_SCALE).astype(BF16)
        k_ref[...] = kn.astype(BF16)
        v_ref[...] = ub_ref[:, 1024:1536].astype(BF16)
        lane = lax.broadcasted_iota(jnp.int32, (1, LANES), 1)
        logf = jnp.where(lane < N_HEADS, _log_sigmoid(ub_ref[:, 2048:2176] + fb_ref[...]), 0.0)
        cum = _exact_dot(logf, _tri(tm, True), ones_first=True) + carry[...]
        for h in range(N_HEADS):
            cc_ref[h] = jnp.broadcast_to(cum[:, h:h + 1], (tm, LANES))
        cr_ref[...] = jnp.transpose(cum)[0:N_HEADS, :]
        carry[...] = cum[tm - 1:tm, :]

    blk = pl.BlockSpec((tm, D_HALF), lambda i: (i, 0))
    return pl.pallas_call(
        body, name="fox_front", grid=(s // tm,),
        in_specs=[pl.BlockSpec((tm, d), lambda i: (i, 0)),
                  pl.BlockSpec(w_b.shape, lambda i: (0, 0), pipeline_mode=pl.Buffered(1)),
                  pl.BlockSpec((1, LANES), lambda i: (0, 0)),
                  pl.BlockSpec((1, D_HALF), lambda i: (0, 0)), pl.BlockSpec((1, D_HALF), lambda i: (0, 0))],
        out_specs=[pl.BlockSpec((tm, SEC), lambda i: (i, 0)), blk, blk, blk,
                   pl.BlockSpec((N_HEADS, tm, LANES), lambda i: (0, i, 0)), pl.BlockSpec((N_HEADS, tm), lambda i: (0, i))],
        out_shape=[jax.ShapeDtypeStruct((s, SEC), F32)] + [jax.ShapeDtypeStruct((s, D_HALF), BF16)] * 3
                  + [jax.ShapeDtypeStruct((N_HEADS, s, LANES), F32), jax.ShapeDtypeStruct((N_HEADS, s), F32)],
        scratch_shapes=[pltpu.VMEM((1, LANES), F32)],
        compiler_params=_params("arbitrary"),
    )(h, w_b, fb, qg, kg)


ATT_T = 256


def _tiles(nblk, by_query):
    if by_query:
        pairs = [(i, j) for i in range(nblk) for j in range(i + 1)]
    else:
        pairs = [(i, j) for j in range(nblk) for i in range(j, nblk)]
    return (jnp.asarray([p[0] for p in pairs], jnp.int32), jnp.asarray([p[1] for p in pairs], jnp.int32))


def _attn_fwd(q, k, v, cc, cr):
    s = q.shape[0]
    t = ATT_T
    nblk = s // t

    def body(qi_ref, kj_ref, q_ref, k_ref, v_ref, cc_ref, cr_ref, o_ref, lse_ref, m_sc, l_sc, acc_sc):
        i = qi_ref[pl.program_id(0)]
        j = kj_ref[pl.program_id(0)]

        @pl.when(j == 0)
        def _():
            m_sc[...] = jnp.full_like(m_sc, NEG)
            l_sc[...] = jnp.zeros_like(l_sc)
            acc_sc[...] = jnp.zeros_like(acc_sc)

        def tile(on_diagonal):
            causal = _causal_tile(t) if on_diagonal else None
            left = lax.broadcasted_iota(jnp.int32, (1, LANES), 1) < HEAD
            for p in range(N_PAIRS):
                lanes = slice(p * LANES, (p + 1) * LANES)
                q2, k2, v2 = q_ref[:, lanes], k_ref[:, lanes], v_ref[:, lanes]
                acc2 = acc_sc[:, lanes]
                for e in range(2):
                    h = 2 * p + e
                    msk = left if e == 0 else jnp.logical_not(left)
                    sc = _dot_nt(jnp.where(msk, q2, jnp.zeros_like(q2)), k2)
                    sc = sc + (_wide(cc_ref[h]) - cr_ref[h:h + 1, :])
                    if on_diagonal:
                        sc = jnp.where(causal, sc, NEG)
                    m_prev = m_sc[h]
                    m_new = jnp.maximum(m_prev, jnp.max(sc, axis=1, keepdims=True))
                    alpha = jnp.exp(m_prev - m_new)
                    pm = jnp.exp(sc - _wide(m_new))
                    l_sc[h] = alpha * l_sc[h] + jnp.sum(pm, axis=1, keepdims=True)
                    m_sc[h] = m_new
                    pv = jnp.dot(pm.astype(BF16), v2, preferred_element_type=F32)
                    acc2 = jnp.where(msk, alpha * acc2 + pv, acc2)
                acc_sc[:, lanes] = acc2

        pl.when(j < i)(functools.partial(tile, False))
        pl.when(j == i)(functools.partial(tile, True))

        @pl.when(j == i)
        def _():
            left = lax.broadcasted_iota(jnp.int32, (1, LANES), 1) < HEAD
            for p in range(N_PAIRS):
                lanes = slice(p * LANES, (p + 1) * LANES)
                inv = jnp.where(left, 1.0 / l_sc[2 * p], 1.0 / l_sc[2 * p + 1])
                o_ref[:, lanes] = acc_sc[:, lanes] * inv
            for h in range(N_HEADS):
                lse_ref[h] = m_sc[h] + jnp.log(l_sc[h])

    qi, kj = _tiles(nblk, by_query=True)
    qblk = pl.BlockSpec((t, D_HALF), lambda n, qi, kj: (qi[n], 0))
    kblk = pl.BlockSpec((t, D_HALF), lambda n, qi, kj: (kj[n], 0))
    qrep = pl.BlockSpec((N_HEADS, t, LANES), lambda n, qi, kj: (0, qi[n], 0))
    return pl.pallas_call(
        body, name="fox_attn_fwd",
        grid_spec=pltpu.PrefetchScalarGridSpec(
            num_scalar_prefetch=2, grid=(qi.shape[0],),
            in_specs=[qblk, kblk, kblk, qrep, pl.BlockSpec((N_HEADS, t), lambda n, qi, kj: (0, kj[n]))],
            out_specs=[qblk, qrep],
            scratch_shapes=[pltpu.VMEM((N_HEADS, t, LANES), F32), pltpu.VMEM((N_HEADS, t, LANES), F32),
                            pltpu.VMEM((t, D_HALF), F32)]),
        out_shape=[jax.ShapeDtypeStruct((s, D_HALF), F32), jax.ShapeDtypeStruct((N_HEADS, s, LANES), F32)],
        compiler_params=_params("arbitrary"),
    )(qi, kj, q, k, v, cc, cr)


def _causal_tile(t):
    return lax.broadcasted_iota(jnp.int32, (t, t), 0) >= lax.broadcasted_iota(jnp.int32, (t, t), 1)


def _wide(x):
    return jnp.concatenate([x, x], axis=1)


def _attn_probs(q2, k2, v2, do2, msk, causal, bias, lse_rows):
    zero = jnp.zeros_like(q2)
    qh = jnp.where(msk, q2, zero)
    doh = jnp.where(msk, do2, zero)
    sc = _dot_nt(qh, k2) + bias
    if causal is not None:
        sc = jnp.where(causal, sc, NEG)
    pm = jnp.exp(sc - _wide(lse_rows))
    dp = _dot_nt(doh, v2)
    return qh, doh, pm, dp


def _attn_bwd_rowdot(q, k, v, do, lse, cc, cr):
    s = q.shape[0]
    t = ATT_T
    nblk = s // t

    def body(qi_ref, kj_ref, q_ref, k_ref, v_ref, do_ref, lse_ref, cc_ref, cr_ref, dd_ref, acc):
        i = qi_ref[pl.program_id(0)]
        j = kj_ref[pl.program_id(0)]

        @pl.when(j == 0)
        def _():
            acc[...] = jnp.zeros_like(acc)

        def tile(on_diagonal):
            causal = _causal_tile(t) if on_diagonal else None
            left = lax.broadcasted_iota(jnp.int32, (1, LANES), 1) < HEAD
            for p in range(N_PAIRS):
                lanes = slice(p * LANES, (p + 1) * LANES)
                q2, k2, v2, do2 = q_ref[:, lanes], k_ref[:, lanes], v_ref[:, lanes], do_ref[:, lanes]
                for e in range(2):
                    h = 2 * p + e
                    msk = left if e == 0 else jnp.logical_not(left)
                    bias = _wide(cc_ref[h]) - cr_ref[h:h + 1, :]
                    _, _, pm, dp = _attn_probs(q2, k2, v2, do2, msk, causal, bias, lse_ref[h])
                    acc[h] += jnp.sum(pm * dp, axis=1, keepdims=True)

        pl.when(j < i)(functools.partial(tile, False))
        pl.when(j == i)(functools.partial(tile, True))

        @pl.when(j == i)
        def _():
            dd_ref[...] = acc[...]

    qi, kj = _tiles(nblk, by_query=True)
    qblk = pl.BlockSpec((t, D_HALF), lambda n, qi, kj: (qi[n], 0))
    qcol = pl.BlockSpec((N_HEADS, t, LANES), lambda n, qi, kj: (0, qi[n], 0))
    kblk = pl.BlockSpec((t, D_HALF), lambda n, qi, kj: (kj[n], 0))
    return pl.pallas_call(
        body, name="fox_attn_rowdot",
        grid_spec=pltpu.PrefetchScalarGridSpec(
            num_scalar_prefetch=2, grid=(qi.shape[0],),
            in_specs=[qblk, kblk, kblk, qblk, qcol, qcol, pl.BlockSpec((N_HEADS, t), lambda n, qi, kj: (0, kj[n]))],
            out_specs=qcol, scratch_shapes=[pltpu.VMEM((N_HEADS, t, LANES), F32)]),
        out_shape=jax.ShapeDtypeStruct((N_HEADS, s, LANES), F32),
        compiler_params=_params("arbitrary"),
    )(qi, kj, q, k, v, do, lse, cc, cr)


def _attn_bwd(q, k, v, do, lse, dd, cc, cr):
    s = q.shape[0]
    t = ATT_T
    nblk = s // t

    def body(qi_ref, kj_ref, q_ref, k_ref, v_ref, do_ref, lse_ref, dd_ref, cc_ref, cr_ref,
             dq_ref, dk_ref, dv_ref, dcr_ref, dk_sc, dv_sc, dcr_sc):
        i = qi_ref[pl.program_id(0)]
        j = kj_ref[pl.program_id(0)]

        @pl.when(pl.program_id(0) == 0)
        def _():
            dq_ref[...] = jnp.zeros_like(dq_ref)

        @pl.when(i == j)
        def _():
            dk_sc[...] = jnp.zeros_like(dk_sc)
            dv_sc[...] = jnp.zeros_like(dv_sc)
            dcr_sc[...] = jnp.zeros_like(dcr_sc)

        def tile(on_diagonal):
            causal = _causal_tile(t) if on_diagonal else None
            left = lax.broadcasted_iota(jnp.int32, (1, LANES), 1) < HEAD
            qrows = pl.ds(pl.multiple_of(i * t, t), t)
            for p in range(N_PAIRS):
                lanes = slice(p * LANES, (p + 1) * LANES)
                q2, k2, v2, do2 = q_ref[:, lanes], k_ref[:, lanes], v_ref[:, lanes], do_ref[:, lanes]
                zero = jnp.zeros_like(q2)
                dq2 = jnp.zeros((t, LANES), F32)
                dk2 = jnp.zeros((t, LANES), F32)
                dv2 = jnp.zeros((t, LANES), F32)
                for e in range(2):
                    h = 2 * p + e
                    msk = left if e == 0 else jnp.logical_not(left)
                    bias = _wide(cc_ref[h]) - cr_ref[h:h + 1, :]
                    qh, doh, pm, dp = _attn_probs(q2, k2, v2, do2, msk, causal, bias, lse_ref[h])
                    dsc = pm * (dp - _wide(dd_ref[h]))
                    dsb = dsc.astype(BF16)
                    dv2 += _dot_tn(pm.astype(BF16), doh)
                    dk2 += _dot_tn(dsb, qh)
                    dq2 += jnp.dot(dsb, jnp.where(msk, k2, zero), preferred_element_type=F32)
                    dcr_sc[h:h + 1, :] += -_colsum(dsc)
                dq_ref[qrows, lanes] += dq2 * ATT_SCALE
                dk_sc[:, lanes] += dk2
                dv_sc[:, lanes] += dv2

        pl.when(i > j)(functools.partial(tile, False))
        pl.when(i == j)(functools.partial(tile, True))

        @pl.when(i == nblk - 1)
        def _():
            dk_ref[...] = dk_sc[...]
            dv_ref[...] = dv_sc[...]
            dcr_ref[...] = dcr_sc[...]

    qi, kj = _tiles(nblk, by_query=False)
    qblk = pl.BlockSpec((t, D_HALF), lambda n, qi, kj: (qi[n], 0))
    qcol = pl.BlockSpec((N_HEADS, t, LANES), lambda n, qi, kj: (0, qi[n], 0))
    kblk = pl.BlockSpec((t, D_HALF), lambda n, qi, kj: (kj[n], 0))
    krow = pl.BlockSpec((N_HEADS, t), lambda n, qi, kj: (0, kj[n]))
    return pl.pallas_call(
        body, name="fox_attn_bwd",
        grid_spec=pltpu.PrefetchScalarGridSpec(
            num_scalar_prefetch=2, grid=(qi.shape[0],),
            in_specs=[qblk, kblk, kblk, qblk, qcol, qcol, qcol, krow],
            out_specs=[pl.BlockSpec((s, D_HALF), lambda n, qi, kj: (0, 0)), kblk, kblk, krow],
            scratch_shapes=[pltpu.VMEM((t, D_HALF), F32), pltpu.VMEM((t, D_HALF), F32), pltpu.VMEM((N_HEADS, t), F32)]),
        out_shape=[jax.ShapeDtypeStruct((s, D_HALF), F32)] * 3 + [jax.ShapeDtypeStruct((N_HEADS, s), F32)],
        compiler_params=_params("arbitrary"),
    )(qi, kj, q, k, v, do, lse, dd, cc, cr)


def _fox_prep_bwd(u_b, h_t, dq, dk, dv, dgate, dcum, fb, qg, kg, tm=256):
    s = u_b.shape[0]
    nb = s // tm
    d = h_t.shape[0]

    def body(ub_ref, ht_ref, dq_ref, dk_ref, dv_ref, dg_ref, dc_ref, fb_ref, qg_ref, kg_ref,
             du_ref, dwb_ref, dqg_ref, dkg_ref, dfb_ref, carry):
        i = pl.program_id(0)

        @pl.when(i == 0)
        def _():
            carry[...] = jnp.zeros_like(carry)
            dwb_ref[...] = jnp.zeros_like(dwb_ref)
            dqg_ref[...] = jnp.zeros_like(dqg_ref)
            dkg_ref[...] = jnp.zeros_like(dkg_ref)
            dfb_ref[...] = jnp.zeros_like(dfb_ref)

        bd = _head_ones()
        for lo, g_ref, d_ref, dgain_ref in ((0, qg_ref, dq_ref, dqg_ref), (512, kg_ref, dk_ref, dkg_ref)):
            gain = g_ref[...]
            xh, rinv, _ = _head_rms(ub_ref[:, lo:lo + 512], gain, bd)
            dn = d_ref[...]
            dgain_ref[...] += _colsum(dn * xh)
            dxh = dn * gain
            du_ref[:, lo:lo + 512] = rinv * (dxh - xh * (_head_sum(dxh * xh, bd) * (1.0 / HEAD)))
        du_ref[:, 1024:1536] = dv_ref[...]
        du_ref[:, 1536:2048] = dg_ref[...]
        lane = lax.broadcasted_iota(jnp.int32, (1, LANES), 1)
        dc = dc_ref[...]
        dlogf = _exact_dot(dc, _tri(tm, False), ones_first=True) + carry[...]
        carry[...] += _colsum(dc)
        fl = ub_ref[:, 2048:2176] + fb_ref[...]
        dfl = jnp.where(lane < N_HEADS, dlogf * (1.0 - _sigmoid(fl)), 0.0)
        du_ref[:, 2048:2176] = dfl
        dfb_ref[...] += _colsum(dfl)
        dwb_ref[...] += jnp.dot(ht_ref[...], du_ref[...].astype(BF16), preferred_element_type=F32)

    rev = lambda w: pl.BlockSpec((tm, w), lambda i: (nb - 1 - i, 0))
    vec = lambda w: pl.BlockSpec((1, w), lambda i: (0, 0))
    return pl.pallas_call(
        body, name="fox_prep_bwd", grid=(nb,),
        in_specs=[rev(SEC), pl.BlockSpec((d, tm), lambda i: (0, nb - 1 - i))] + [rev(D_HALF)] * 4
                 + [rev(LANES), vec(LANES), vec(D_HALF), vec(D_HALF)],
        out_specs=[rev(SEC), pl.BlockSpec((d, SEC), lambda i: (0, 0)), vec(D_HALF), vec(D_HALF), vec(LANES)],
        out_shape=[jax.ShapeDtypeStruct((s, SEC), F32), jax.ShapeDtypeStruct((d, SEC), F32),
                   jax.ShapeDtypeStruct((1, D_HALF), F32), jax.ShapeDtypeStruct((1, D_HALF), F32),
                   jax.ShapeDtypeStruct((1, LANES), F32)],
        scratch_shapes=[pltpu.VMEM((1, LANES), F32)],
        compiler_params=_params("arbitrary"),
    )(u_b, h_t, dq, dk, dv, dgate, dcum, fb, qg, kg)


def _merge(y, r, k, v, gate_a, o, u_b, h, x, tgt, w_g, wa, wb, wo, fg, lw, lb, rk, tm=256):
    s, d = x.shape

    def body(y_ref, r_ref, k_ref, v_ref, ga_ref, o_ref, gb_ref, h_ref, x_ref, t_ref, wg_ref, wa_ref, wb_ref, wo_ref,
             fg_ref, lw_ref, lb_ref, rk_ref,
             dx2_ref, dy_ref, drb_ref, dkb_ref, dvb_ref, dga_ref, do_ref, dgb_ref, dug_ref,
             dwa_ref, dwb_ref, dwo_ref, dfg_ref, loss_ref, dlw_ref, dlb_ref, drk_ref):
        i = pl.program_id(0)

        @pl.when(i == 0)
        def _():
            for ref in (dwa_ref, dwb_ref, dwo_ref, dfg_ref, loss_ref, dlw_ref, dlb_ref, drk_ref):
                ref[...] = jnp.zeros_like(ref)

        bd = _head_ones()
        wa_v, wb_v, wo_v, fg_v = wa_ref[...], wb_ref[...], wo_ref[...], fg_ref[...]
        rv, kv, vv, ga, lw_v, rk_v = r_ref[...], k_ref[...], v_ref[...], ga_ref[...], lw_ref[...], rk_ref[...]
        yn, rstd, rkk, sga, pre = _rwkv_post_math(y_ref[...], rv, kv, vv, ga, lw_v, lb_ref[...], rk_v, bd)
        silu_a = ga * sga
        gb, ov = gb_ref[...], o_ref[...]
        sgb = _sigmoid(gb)
        silu_b = gb * sgb
        ma = (pre * silu_a).astype(BF16)
        mb = (ov * silu_b).astype(BF16)
        ya = jnp.dot(ma, wa_v, preferred_element_type=F32)
        yb = jnp.dot(mb, wb_v, preferred_element_type=F32)
        ug = jnp.dot(h_ref[...], wg_ref[...], preferred_element_type=F32)
        sa = _sigmoid(ug[:, 0:d])
        sb = _sigmoid(ug[:, d:2 * d])
        merged = (sa * ya + sb * yb).astype(BF16)
        x2 = x_ref[...] + jnp.dot(merged, wo_v, preferred_element_type=F32)
        r2 = lax.rsqrt(jnp.mean(x2 * x2, axis=-1, keepdims=True) + RMS_EPS)
        x2h = x2 * r2
        err = x2h * fg_v - t_ref[...]
        loss_ref[...] += _colsum(err * err)
        dyo = err * (1.0 / d)
        dfg_ref[...] += _colsum(dyo * x2h)
        dx2h = dyo * fg_v
        dx2 = r2 * (dx2h - x2h * jnp.mean(dx2h * x2h, axis=-1, keepdims=True))
        dx2_ref[...] = dx2
        dx2b = dx2.astype(BF16)
        dmerged = _dot_nt(dx2b, wo_v)
        dwo_ref[...] += _dot_tn(merged, dx2b)
        dya = dmerged * sa
        dyb = dmerged * sb
        dug_ref[:, 0:d] = dya * ya * (1.0 - sa)
        dug_ref[:, d:2 * d] = dyb * yb * (1.0 - sb)
        dyab = dya.astype(BF16)
        dybb = dyb.astype(BF16)
        dwa_ref[...] += _dot_tn(ma, dyab)
        dwb_ref[...] += _dot_tn(mb, dybb)
        dmb = _dot_nt(dybb, wb_v)
        do_ref[...] = (dmb * silu_b).astype(BF16)
        dgb_ref[...] = dmb * ov * (sgb * (1.0 + gb * (1.0 - sgb)))
        dma = _dot_nt(dyab, wa_v)
        dga_ref[...] = dma * pre * (sga * (1.0 + ga * (1.0 - sga)))
        dpre = dma * silu_a
        dlw_ref[...] += _colsum(dpre * yn)
        dlb_ref[...] += _colsum(dpre)
        dyn = dpre * lw_v
        m1 = _head_sum(dyn, bd) * (1.0 / HEAD)
        m2 = _head_sum(dyn * yn, bd) * (1.0 / HEAD)
        dy_ref[...] = rstd * (dyn - m1 - yn * m2)
        dvb_ref[...] = dpre * rkk
        drkk = _head_sum(dpre * vv, bd)
        drb_ref[...] = drkk * kv * rk_v
        dkb_ref[...] = drkk * rv * rk_v
        drk_ref[...] += _colsum(drkk * rv * kv)

    row = lambda w: pl.BlockSpec((tm, w), lambda i: (i, 0))
    full = lambda a: pl.BlockSpec(a.shape, lambda i: (0, 0))
    once = lambda a: pl.BlockSpec(a.shape, lambda i: (0, 0), pipeline_mode=pl.Buffered(1))
    half = jax.ShapeDtypeStruct((s, D_HALF), F32)
    fshape = lambda a: jax.ShapeDtypeStruct(a.shape, F32)
    return pl.pallas_call(
        body, name="merge_fwd_bwd", grid=(s // tm,),
        in_specs=[row(D_HALF)] * 6 + [pl.BlockSpec((tm, D_HALF), lambda i: (i, 3)), row(d), row(d), row(d),
                                      once(w_g), once(wa), once(wb), once(wo), full(fg), full(lw), full(lb), full(rk)],
        out_specs=[row(d)] + [row(D_HALF)] * 7 + [row(GATE_COLS), full(wa), full(wb), full(wo), full(fg), full(fg),
                                                   full(lw), full(lb), full(rk)],
        out_shape=[jax.ShapeDtypeStruct((s, d), F32)] + [half] * 5 + [jax.ShapeDtypeStruct((s, D_HALF), BF16), half,
                                                                    jax.ShapeDtypeStruct((s, GATE_COLS), F32),
                                                                    fshape(wa), fshape(wb), fshape(wo), fshape(fg),
                                                                    fshape(fg), fshape(lw), fshape(lb), fshape(rk)],
        compiler_params=_params("arbitrary"),
    )(y, r, k, v, gate_a, o, u_b, h, x, tgt, w_g, wa, wb, wo, fg, lw, lb, rk)


def _lora_weight(w_up, a_up):
    z = jnp.zeros((LORA, D_HALF), w_up.dtype)
    return jnp.concatenate([jnp.concatenate([w_up, z], axis=1), jnp.concatenate([z, a_up], axis=1)], axis=0)


def _device_grads(x, tgt, p, w_a, w_up, a_up, late_weights, fwd_exchange=None, bwd_exchange=None, tail_exchange=None):
    wl = _lora_weight(w_up, a_up)
    rk = p["r_k"].reshape(1, D_HALF)
    fb = jnp.pad(p["f_bias"], ((0, 0), (0, LANES - N_HEADS)))
    qg = jnp.tile(p["q_norm_g"], (1, N_HEADS))
    kg = jnp.tile(p["k_norm_g"], (1, N_HEADS))
    fg = p["final_norm_g"].reshape(1, D_MODEL)
    mixer = (p["shift_mu"], wl, p["w0"], p["a0"], p["k_k"], p["k_a"])

    h, u_a, r, dec, k, v, av, bv, gate_a = _rwkv_front(x, p["norm_g"], w_a, *mixer)
    y, st, arrived = _wkv_fwd(r, dec, k, av, bv, v, fwd_exchange)

    w_b, w_g, w_out_a, w_out_b, w_out = late_weights(arrived)
    u_b, q, kn, vb, cc, cr = _fox_front(h, w_b, fb, qg, kg)
    o, lse = _attn_fwd(q, kn, vb, cc, cr)

    (dx2, dy, dr_b, dk_b, dv_b, dgate_a, do, dgate_b, du_g, dwa, dwb, dwo, dfg, loss_vec, dlw, dlb, drk) = _merge(
        y, r, k, v, gate_a, o, u_b, h, x, tgt, w_g, w_out_a, w_out_b, w_out, fg, p["lnx_w"], p["lnx_b"], rk)

    dd = _attn_bwd_rowdot(q, kn, vb, do, lse, cc, cr)
    dq, dk_att, dv_att, dcr = _attn_bwd(q, kn, vb, do, lse, dd, cc, cr)
    dcum = jnp.pad(dcr.T, ((0, 0), (0, LANES - N_HEADS)))
    h_t = h.T
    du_b, dw_b, dqg, dkg, dfb = _fox_prep_bwd(u_b, h_t, dq, dk_att, dv_att, dgate_b, dcum, fb, qg, kg)
    dw_g = _matmul_tn_acc(h_t, du_g, "dw_gate")

    scan_grads, sent = _wkv_bwd(r, dec, k, av, bv, v, dy, st,
                                bwd_exchange(dw_b, dw_g, dwa, dwb, dwo) if bwd_exchange else None)
    du_a, dw_a, dmu, dwl, dw0, da0, dkkw, dkaw = _rwkv_prep_bwd(
        u_a, h_t, (*scan_grads, dr_b, dk_b, dv_b, dgate_a), *mixer)
    dw_up, da_up = dwl[:LORA, :D_HALF], dwl[LORA:, D_HALF:]
    sent_last = _run_on_sequencer(tail_exchange(dw_a, dw_up, da_up), "scatter_tail", 1) if tail_exchange else []
    grad_x, dnorm_g, _ = _inproj_bwd(du_a, du_b, du_g, w_a, w_b, w_g, x, dx2, p["norm_g"])

    grads = dict(
        norm_g=dnorm_g, w_in=(dw_a, dw_b, dw_g), shift_mu=dmu,
        w_lora_up=dw_up, w0=dw0, a_lora_up=da_up, a0=da0, k_k=dkkw, k_a=dkaw,
        r_k=drk.reshape(1, N_HEADS, HEAD), lnx_w=dlw, lnx_b=dlb, f_bias=dfb[:, :N_HEADS],
        q_norm_g=dqg.reshape(N_HEADS, HEAD).sum(axis=0, keepdims=True),
        k_norm_g=dkg.reshape(N_HEADS, HEAD).sum(axis=0, keepdims=True),
        w_out_a=dwa, w_out_b=dwb, w_out=dwo, final_norm_g=dfg.reshape(D_MODEL))
    return loss_vec, grad_x, grads, sent, sent_last


CHIP_FLIPS = ((1, 0), (0, 1), (1, 1))
ANY = pl.BlockSpec(memory_space=pl.ANY)


def _position():
    return lax.axis_index("x"), lax.axis_index("y"), lax.axis_index("c")


def _flip(v, f):
    return 1 - v if f else v


def _both(a, b):
    if a is None:
        return b
    return a if b is None else jnp.logical_and(a, b)


def _when(cond, fn):
    if cond is None:
        fn()
    else:
        pl.when(cond)(fn)


class _Moves:
    def __init__(self, send_sems, recv_sems, local_sems):
        self.send_sems, self.recv_sems, self.local_sems = send_sems, recv_sems, local_sems
        self.remote, self.local = [], []

    def send(self, src, dst, peer, landing, send_if=None, recv_if=None, first=False):
        k = len(self.remote)
        sems = dict(send_sem=self.send_sems.at[k], recv_sem=self.recv_sems.at[k], device_id=peer, device_id_type=MESH)
        out = pltpu.make_async_remote_copy(src_ref=src, dst_ref=dst, **sems)
        arrival = pltpu.make_async_remote_copy(src_ref=src, dst_ref=landing, **sems)
        self.remote.append((out, arrival, send_if, recv_if, first))

    def copy(self, src, dst, cond=None):
        cp = pltpu.make_async_copy(src, dst, self.local_sems.at[len(self.local)])
        self.local.append((cp, cond))

    def start(self, also=None):
        for cp, cond in self.local:
            _when(_both(also, cond), cp.start)
        for out, _, send_if, _, _ in self.remote:
            _when(_both(also, send_if), out.start)

    def wait_arrivals(self, also=None, first=None):
        for _, arrival, _, recv_if, is_first in self.remote:
            if first is None or first == is_first:
                _when(_both(also, recv_if), arrival.wait_recv)

    def wait_sent(self, also=None):
        for out, _, send_if, _, _ in self.remote:
            _when(_both(also, send_if), out.wait_send)
        for cp, cond in self.local:
            _when(_both(also, cond), cp.wait)

    def wait(self, also=None):
        self.wait_arrivals(also)
        self.wait_sent(also)


class _Exchange:
    def __init__(self, operands, out_shapes, n_remote, n_local, build, relays=None):
        self.operands, self.out_shapes = list(operands), list(out_shapes)
        self.n_remote, self.n_local, self.build = n_remote, n_local, build
        self.relays = relays

    def scratch(self):
        return [pltpu.SemaphoreType.DMA((self.n_remote,)), pltpu.SemaphoreType.DMA((self.n_remote,)),
                pltpu.SemaphoreType.DMA((max(self.n_local, 1),))]

    def moves(self, in_refs, out_refs, sems):
        mv = _Moves(*sems)
        self.build(mv, in_refs, out_refs)
        return mv


def _run_on_sequencer(exchange, name, collective_id):
    ins = [jax.new_ref(a, memory_space=pltpu.MemorySpace.HBM) for a in exchange.operands]
    outs = [jax.empty_ref(s, memory_space=pltpu.MemorySpace.HBM) for s in exchange.out_shapes]
    relay_scratch = [pltpu.SemaphoreType.DMA((n,)) for n, _ in exchange.relays or () for _ in range(2)]

    def launch(*sems):
        x, y, c = _position()
        peers = [(_flip(x, fx), _flip(y, fy), c) for fx, fy in CHIP_FLIPS] + ([(x, y, 1 - c)] if exchange.relays else [])
        barrier = pltpu.get_barrier_semaphore()
        for peer in peers:
            pl.semaphore_signal(barrier, inc=1, device_id=peer, device_id_type=MESH)
        pl.semaphore_wait(barrier, len(peers))
        moves = exchange.moves(ins, outs, sems[:3])
        moves.start()
        if exchange.relays:
            (_, forward), (_, to_sibling) = exchange.relays
            onward, passed = _Moves(sems[3], sems[4], None), _Moves(sems[5], sems[6], None)
            forward(onward, ins, outs)
            to_sibling(passed, ins, outs)
            moves.wait_arrivals(first=True)
            onward.start()
            moves.wait_arrivals(first=False)
            onward.wait_arrivals()
            passed.start()
            passed.wait()
            onward.wait_sent()
        else:
            moves.wait_arrivals()
        moves.wait_sent()

    pl.kernel(launch, mesh=plsc.ScalarSubcoreMesh(axis_name="sequencer", num_cores=1), name=name,
              scratch_types=tuple(exchange.scratch() + relay_scratch),
              compiler_params=pltpu.CompilerParams(collective_id=collective_id))()
    return [o[...] for o in outs]


def _row_major_copy(a, name):
    r, c = a.shape
    tr = _row_tile(r)

    def body(a_ref, o_ref):
        o_ref[...] = a_ref[...]

    blk = pl.BlockSpec((tr, c), lambda i: (i, 0))
    return pl.pallas_call(body, name=name, grid=(r // tr,), in_specs=[blk], out_specs=blk,
                          out_shape=jax.ShapeDtypeStruct(a.shape, a.dtype), compiler_params=_params("parallel"))(a)


def _is_chip(x, y, chip):
    return jnp.logical_and(x == chip // 2, y == chip % 2)


def _gather_exchange(from_chip, from_all, split=()):
    n1, n2 = len(from_chip), len(from_all)
    near = CHIP_FLIPS[:2]

    def quarters(t, c, first, count=1):
        n = from_chip[t][1].shape[0] // 4
        return pl.ds((2 * c + first) * n, count * n)

    def build(mv, ins, outs):
        x, y, c = _position()
        me = 2 * x + y
        for t, (chip, _) in enumerate(from_chip):
            mv.copy(ins[t], outs[t], cond=_is_chip(x, y, chip))
        for t in range(n2):
            mv.copy(ins[n1 + t], outs[n1 + t].at[me])
        for t in split:
            for first in (True, False):
                for f, (fx, fy) in enumerate(near):
                    px, py = _flip(x, fx), _flip(y, fy)
                    part = quarters(t, c, f if first else 1 - f)
                    mv.send(ins[t].at[part], outs[t].at[part], (px, py, c), landing=outs[t].at[part], first=first,
                            send_if=_is_chip(x, y, from_chip[t][0]), recv_if=_is_chip(px, py, from_chip[t][0]))
        for fx, fy in CHIP_FLIPS:
            px, py = _flip(x, fx), _flip(y, fy)
            peer = (px, py, c)
            for t, (chip, _) in enumerate(from_chip):
                if t not in split:
                    mv.send(ins[t], outs[t], peer, landing=outs[t],
                            send_if=_is_chip(x, y, chip), recv_if=_is_chip(px, py, chip))
            for t in range(n2):
                mv.send(ins[n1 + t], outs[n1 + t].at[me], peer, landing=outs[n1 + t].at[2 * px + py])

    def forward(mv, ins, outs):
        x, y, c = _position()
        for t in split:
            chip = from_chip[t][0]
            for f, (fx, fy) in enumerate(near):
                gx, gy = near[1 - f]
                part = quarters(t, c, f)
                mv.send(outs[t].at[part], outs[t].at[part], (_flip(x, gx), _flip(y, gy), c), landing=outs[t].at[part],
                        send_if=_is_chip(_flip(x, fx), _flip(y, fy), chip), recv_if=_is_chip(1 - x, 1 - y, chip))

    def to_sibling(mv, ins, outs):
        x, y, c = _position()
        for t in split:
            came = jnp.logical_not(_is_chip(x, y, from_chip[t][0]))
            mv.send(outs[t].at[quarters(t, c, 0, 2)], outs[t].at[quarters(t, c, 0, 2)], (x, y, 1 - c),
                    landing=outs[t].at[quarters(t, 1 - c, 0, 2)], send_if=came, recv_if=came)

    arrays = [a for _, a in from_chip] + list(from_all)
    shapes = [jax.ShapeDtypeStruct(a.shape, a.dtype) for _, a in from_chip]
    shapes += [jax.ShapeDtypeStruct((N_CHIPS,) + a.shape, a.dtype) for a in from_all]
    n_remote = len(CHIP_FLIPS) * (n1 - len(split) + n2) + 2 * len(near) * len(split)
    relays = ((len(near) * len(split), forward), (len(split), to_sibling)) if split else None
    return _Exchange(arrays, shapes, n_remote, n1 + n2, build, relays)


def _scatter_exchange(to_chip, to_all):
    n1, n2 = len(to_chip), len(to_all)

    def build(mv, ins, outs):
        x, y, c = _position()
        for f, (fx, fy) in enumerate(CHIP_FLIPS):
            px, py = _flip(x, fx), _flip(y, fy)
            peer = (px, py, c)
            for t, (chip, _) in enumerate(to_chip):
                mv.send(ins[t], outs[t].at[f], peer, landing=outs[t].at[f],
                        send_if=_is_chip(px, py, chip), recv_if=_is_chip(x, y, chip))
            for t in range(n2):
                mv.send(ins[n1 + t].at[2 * px + py], outs[n1 + t].at[f], peer, landing=outs[n1 + t].at[f])

    arrays = [a for _, a in to_chip] + list(to_all)
    shapes = [jax.ShapeDtypeStruct((len(CHIP_FLIPS),) + a.shape, a.dtype) for _, a in to_chip]
    shapes += [jax.ShapeDtypeStruct((len(CHIP_FLIPS),) + a.shape[1:], a.dtype) for a in to_all]
    return _Exchange(arrays, shapes, len(CHIP_FLIPS) * (n1 + n2), 0, build)


def _swap_sibling(tensors, name):
    n = len(tensors)

    def body(*refs):
        ins, outs = refs[:n], refs[n:2 * n]
        send_sems, recv_sems = refs[2 * n:]
        x, y, c = _position()
        copies = [pltpu.make_async_remote_copy(
            src_ref=ins[t], dst_ref=outs[t], send_sem=send_sems.at[t], recv_sem=recv_sems.at[t],
            device_id=(x, y, 1 - c), device_id_type=MESH) for t in range(n)]
        for cp in copies:
            cp.start()
        for cp in copies:
            cp.wait_recv()
        for cp in copies:
            cp.wait_send()

    return pl.pallas_call(
        body, name=name, in_specs=[ANY] * n, out_specs=[ANY] * n,
        out_shape=[jax.ShapeDtypeStruct(a.shape, a.dtype) for a in tensors],
        scratch_shapes=[pltpu.SemaphoreType.DMA((n,)), pltpu.SemaphoreType.DMA((n,))],
        compiler_params=pltpu.CompilerParams(has_side_effects=True),
    )(*tensors)


def _allreduce_small(slab):
    stages = 3

    def body(x_ref, o_ref, buf, send_sems, recv_sems):
        x, y, c = _position()
        peers = ((1 - x, y, c), (x, 1 - y, c), (x, y, 1 - c))
        o_ref[...] = x_ref[...]
        for k, peer in enumerate(peers):
            cp = pltpu.make_async_remote_copy(src_ref=o_ref, dst_ref=buf.at[k], send_sem=send_sems.at[k],
                                              recv_sem=recv_sems.at[k], device_id=peer, device_id_type=MESH)
            cp.start()
            cp.wait()
            o_ref[...] = o_ref[...] + buf[k]

    return pl.pallas_call(
        body, name="allreduce_small",
        in_specs=[pl.BlockSpec(memory_space=pltpu.VMEM)], out_specs=pl.BlockSpec(memory_space=pltpu.VMEM),
        out_shape=jax.ShapeDtypeStruct(slab.shape, slab.dtype),
        scratch_shapes=[pltpu.VMEM((stages,) + slab.shape, slab.dtype),
                        pltpu.SemaphoreType.DMA((stages,)), pltpu.SemaphoreType.DMA((stages,))],
        compiler_params=pltpu.CompilerParams(has_side_effects=True),
    )(slab)


def _row_tile(r):
    return min(r, 256)


def _sum4(stack, recv, me):
    _, r, c = stack.shape
    tr = _row_tile(r)

    def body(me_ref, own_ref, recv_ref, o_ref):
        o_ref[...] = (((own_ref[...] + recv_ref[0].astype(F32)) + recv_ref[1].astype(F32))
                      + recv_ref[2].astype(F32))

    return pl.pallas_call(
        body, name="sum_partials",
        grid_spec=pltpu.PrefetchScalarGridSpec(
            num_scalar_prefetch=1, grid=(r // tr,),
            in_specs=[pl.BlockSpec((None, tr, c), lambda i, me_ref: (me_ref[0], i, 0)),
                      pl.BlockSpec((len(CHIP_FLIPS), tr, c), lambda i, me_ref: (0, i, 0))],
            out_specs=pl.BlockSpec((tr, c), lambda i, me_ref: (i, 0))),
        out_shape=jax.ShapeDtypeStruct((r, c), F32), compiler_params=_params("parallel"),
    )(me, stack, recv)


def _sum_block(own, recv):
    r, c = own.shape
    tr = _row_tile(r)

    def body(own_ref, recv_ref, o_ref):
        o_ref[...] = (((own_ref[...] + recv_ref[0].astype(F32)) + recv_ref[1].astype(F32))
                      + recv_ref[2].astype(F32))

    return pl.pallas_call(
        body, name="sum_block", grid=(r // tr,),
        in_specs=[pl.BlockSpec((tr, c), lambda i: (i, 0)), pl.BlockSpec((len(CHIP_FLIPS), tr, c), lambda i: (0, i, 0))],
        out_specs=pl.BlockSpec((tr, c), lambda i: (i, 0)),
        out_shape=jax.ShapeDtypeStruct((r, c), F32), compiler_params=_params("parallel"),
    )(own, recv)


def _adamw_math(w, g, m, v):
    m = ADAM_B1 * m + (1.0 - ADAM_B1) * g
    v = ADAM_B2 * v + (1.0 - ADAM_B2) * (g * g)
    m_hat = m / (1.0 - ADAM_B1 ** ADAM_STEP)
    v_hat = v / (1.0 - ADAM_B2 ** ADAM_STEP)
    delta = -ADAM_LR * (m_hat / (jnp.sqrt(v_hat) + ADAM_EPS) + ADAM_WD * w)
    return delta, m, v


def _adamw(w, m, v, g_parts, name):
    r, c = w.shape
    tr = _row_tile(r)
    n = len(g_parts)

    def body(*refs):
        w_ref, m_ref, v_ref = refs[:3]
        g_refs = refs[3:3 + n]
        g_out, d_out, m_out, v_out = refs[3 + n:]
        g = g_refs[0][...]
        for ref in g_refs[1:]:
            g = g + ref[...]
        g_out[...] = g
        d_out[...], m_out[...], v_out[...] = _adamw_math(w_ref[...], g, m_ref[...], v_ref[...])

    blk = pl.BlockSpec((tr, c), lambda i: (i, 0))
    return pl.pallas_call(
        body, name=name, grid=(r // tr,), in_specs=[blk] * (3 + n), out_specs=[blk] * 4,
        out_shape=[jax.ShapeDtypeStruct((r, c), F32)] * 4, compiler_params=_params("parallel"),
    )(w, m, v, *g_parts)


def _adamw_small(total, w, m, v):
    sizes = [w[n].size for n in SMALL]
    flat = lambda d: [d[n].reshape(1, -1) for n in SMALL]
    k = len(SMALL)

    def body(*refs):
        total_ref, w_refs, m_refs, v_refs = refs[0], refs[1:1 + k], refs[1 + k:1 + 2 * k], refs[1 + 2 * k:1 + 3 * k]
        outs = refs[1 + 3 * k:]
        for i, size in enumerate(sizes):
            g = total_ref[i:i + 1, 0:size]
            outs[i][...] = g
            outs[k + i][...], outs[2 * k + i][...], outs[3 * k + i][...] = _adamw_math(
                w_refs[i][...], g, m_refs[i][...], v_refs[i][...])

    res = pl.pallas_call(
        body, name="adamw_small", out_shape=[jax.ShapeDtypeStruct((1, size), F32) for size in sizes] * 4,
        compiler_params=_params(),
    )(total, *flat(w), *flat(m), *flat(v))
    return [{n: res[j * k + i].reshape(w[n].shape) for i, n in enumerate(SMALL)} for j in range(4)]


SHARDED = ("w_in", "w_lora_up", "a_lora_up", "w_out_a", "w_out_b", "w_out")
ROW_SHARDED = ("w_out",)
SMALL = ("norm_g", "shift_mu", "w0", "a0", "k_k", "k_a", "r_k", "lnx_w", "lnx_b", "f_bias", "q_norm_g", "k_norm_g",
         "final_norm_g")
WEIGHTS = ("norm_g", "w_in", "shift_mu", "w_lora_up", "w0", "a_lora_up", "a0", "k_k", "k_a", "r_k", "lnx_w", "lnx_b",
           "f_bias", "q_norm_g", "k_norm_g", "w_out_a", "w_out_b", "w_out", "final_norm_g")
SLAB_ROWS = 16
SLAB_COLS = SEC


def _to_slab(named, extra=None):
    rows = [jnp.pad(named[n].reshape(1, -1), ((0, 0), (0, SLAB_COLS - named[n].size))) for n in SMALL]
    if extra is not None:
        rows.append(jnp.pad(extra.reshape(1, -1), ((0, 0), (0, SLAB_COLS - extra.size))))
    rows.append(jnp.zeros((SLAB_ROWS - len(rows), SLAB_COLS), F32))
    return jnp.concatenate(rows, axis=0)


def _by_chip(g, name):
    if name in ROW_SHARDED:
        return g.reshape(N_CHIPS, g.shape[0] // N_CHIPS, g.shape[1])
    r, c = g.shape
    return g.reshape(r, N_CHIPS, c // N_CHIPS).transpose(1, 0, 2)


def _from_chips(stack, name):
    if name in ROW_SHARDED:
        return stack.reshape(-1, stack.shape[2])
    _, r, c = stack.shape
    return stack.transpose(1, 0, 2).reshape(r, N_CHIPS * c)


def kernel(x, norm_g, w_in, shift_mu, w_lora_up, w0, a_lora_up, a0, k_k, k_a, r_k, lnx_w, lnx_b, f_bias, q_norm_g, k_norm_g, w_out_a, w_out_b, w_out, final_norm_g, loss_target, m_norm_g, m_w_in, m_shift_mu, m_w_lora_up, m_w0, m_a_lora_up, m_a0, m_k_k, m_k_a, m_r_k, m_lnx_w, m_lnx_b, m_f_bias, m_q_norm_g, m_k_norm_g, m_w_out_a, m_w_out_b, m_w_out, m_final_norm_g, v_norm_g, v_w_in, v_shift_mu, v_w_lora_up, v_w0, v_a_lora_up, v_a0, v_k_k, v_k_a, v_r_k, v_lnx_w, v_lnx_b, v_f_bias, v_q_norm_g, v_k_norm_g, v_w_out_a, v_w_out_b, v_w_out, v_final_norm_g):
    w = dict(norm_g=norm_g, w_in=w_in, shift_mu=shift_mu, w_lora_up=w_lora_up, w0=w0, a_lora_up=a_lora_up, a0=a0,
             k_k=k_k, k_a=k_a, r_k=r_k, lnx_w=lnx_w, lnx_b=lnx_b, f_bias=f_bias, q_norm_g=q_norm_g,
             k_norm_g=k_norm_g, w_out_a=w_out_a, w_out_b=w_out_b, w_out=w_out, final_norm_g=final_norm_g)
    m = dict(norm_g=m_norm_g, w_in=m_w_in, shift_mu=m_shift_mu, w_lora_up=m_w_lora_up, w0=m_w0,
             a_lora_up=m_a_lora_up, a0=m_a0, k_k=m_k_k, k_a=m_k_a, r_k=m_r_k, lnx_w=m_lnx_w, lnx_b=m_lnx_b,
             f_bias=m_f_bias, q_norm_g=m_q_norm_g, k_norm_g=m_k_norm_g, w_out_a=m_w_out_a, w_out_b=m_w_out_b,
             w_out=m_w_out, final_norm_g=m_final_norm_g)
    v = dict(norm_g=v_norm_g, w_in=v_w_in, shift_mu=v_shift_mu, w_lora_up=v_w_lora_up, w0=v_w0,
             a_lora_up=v_a_lora_up, a0=v_a0, k_k=v_k_k, k_a=v_k_a, r_k=v_r_k, lnx_w=v_lnx_w, lnx_b=v_lnx_b,
             f_bias=v_f_bias, q_norm_g=v_q_norm_g, k_norm_g=v_k_norm_g, w_out_a=v_w_out_a, w_out_b=v_w_out_b,
             w_out=v_w_out, final_norm_g=v_final_norm_g)
    shapes = {n: w[n].shape for n in WEIGHTS}

    shard = {n: w[n][0].astype(BF16) for n in SHARDED}
    late = ("w_out_a", "w_out_b", "w_out")
    loras = ("w_lora_up", "a_lora_up")
    w_in_head, w_in_tail = shard["w_in"][:, :A_TAIL], shard["w_in"][:, A_TAIL:]
    shard0, shard1_head, up_stack, aup_stack = _run_on_sequencer(_gather_exchange(
        [(0, shard["w_in"]), (1, w_in_head)], [shard[n] for n in loras], split=(0, 1)), "gather_early", 2)
    moments = (_row_major_copy(m["w_in"][0], "m_w_in_rows"), _row_major_copy(v["w_in"][0], "v_w_in_rows"))
    shard0, moments = lax.optimization_barrier((shard0, moments))
    w_a = jnp.concatenate([shard0, shard1_head], axis=1)

    def late_weights(arrived):
        shard1_tail, shard2, shard3 = arrived[:3]
        w_b = jnp.concatenate([shard1_tail, shard2[:, :B_TAIL], jnp.zeros((D_MODEL, SEC - FOX_REAL), BF16)], axis=1)
        w_g = jnp.concatenate([shard2[:, B_TAIL:], shard3], axis=1)
        return (w_b, w_g, *[_from_chips(s, n) for n, s in zip(late, arrived[3:])])

    own = {}

    def bwd_exchange(dw_b, dw_g, dwa, dwb, dwo):
        own["tail1"] = dw_b[:, :B_HEAD]
        own["block2"] = jnp.concatenate([dw_b[:, B_HEAD:FOX_REAL], dw_g[:, :G_HEAD]], axis=1)
        own["block3"] = dw_g[:, G_HEAD:]
        own.update({n: _by_chip(g, n) for n, g in zip(late, (dwa, dwb, dwo))})
        return _scatter_exchange([(1, own["tail1"].astype(BF16)), (2, own["block2"].astype(BF16)),
                                  (3, own["block3"].astype(BF16))], [own[n].astype(BF16) for n in late])

    def tail_exchange(dw_a, dw_up, da_up):
        own["block0"], own["head1"] = dw_a[:, :SHARD_COLS], dw_a[:, SHARD_COLS:]
        own.update({n: _by_chip(g, n) for n, g in zip(loras, (dw_up, da_up))})
        return _scatter_exchange([(0, own["block0"].astype(BF16)), (1, own["head1"].astype(BF16))],
                                 [own[n].astype(BF16) for n in loras])

    small = {n: w[n] for n in SMALL}
    loss_vec, grad_x, grads, sent, sent_last = _device_grads(
        x[0], loss_target[0], small, w_a, _from_chips(up_stack, "w_lora_up"), _from_chips(aup_stack, "a_lora_up"),
        late_weights, _gather_exchange([(1, w_in_tail), (2, shard["w_in"]), (3, shard["w_in"])], [shard[n] for n in late]),
        bwd_exchange, tail_exchange)

    total = _allreduce_small(_to_slab(grads, extra=loss_vec))
    loss = (0.5 / D_MODEL) * jnp.sum(total[len(SMALL)])
    out_g, out_d, out_m, out_v = _adamw_small(total, w, m, v)

    xpos, ypos, _ = _position()
    me = (2 * xpos + ypos).astype(jnp.int32).reshape(1)
    core_sum = {n: _sum4(own[n], r, me) for n, r in zip(late, sent[3:])}
    theirs = dict(zip(late, _swap_sibling([core_sum[n] for n in late], "swap_sibling_early")))
    sent_last, out_d["norm_g"], theirs = lax.optimization_barrier((sent_last, out_d["norm_g"], theirs))
    core_sum["w_in"] = lax.switch(me[0], [
        lambda: _sum_block(own["block0"], sent_last[0]),
        lambda: jnp.concatenate([_sum_block(own["head1"], sent_last[1]), _sum_block(own["tail1"], sent[0])], axis=1),
        lambda: _sum_block(own["block2"], sent[1]),
        lambda: _sum_block(own["block3"], sent[2])])
    core_sum.update({n: _sum4(own[n], r, me) for n, r in zip(loras, sent_last[2:])})
    rest = ("w_in",) + loras
    theirs.update(zip(rest, _swap_sibling([core_sum[n] for n in rest], "swap_sibling")))
    for n in SHARDED:
        m_n, v_n = moments if n == "w_in" else (m[n][0], v[n][0])
        g, d, m2, v2 = _adamw(w[n][0], m_n, v_n, [core_sum[n], theirs[n]], "adamw_" + n)
        out_g[n], out_d[n], out_m[n], out_v[n] = (a.reshape(shapes[n]) for a in (g, d, m2, v2))

    return (loss, grad_x.reshape(x.shape), *[out_g[n] for n in WEIGHTS], *[out_d[n] for n in WEIGHTS],
            *[out_m[n] for n in WEIGHTS], *[out_v[n] for n in WEIGHTS])
```

```python
import functools
import math

import jax
import jax.numpy as jnp
from jax import lax
from jax.experimental import pallas as pl
from jax.experimental.pallas import tpu as pltpu
from jax.experimental.pallas import tpu_sc as plsc

F32 = jnp.float32
BF16 = jnp.bfloat16

D_MODEL = 1024
D_HALF = 512
HEAD = 64
N_HEADS = 8
LORA = 64
RWKV_COLS = 2176
FOX_REAL = 2056
SEC = 2176
GATE_COLS = 2048
IN_COLS = 6280
N_CHIPS = 4
SHARD_COLS = IN_COLS // N_CHIPS
A_TAIL = RWKV_COLS - SHARD_COLS
B_HEAD = SHARD_COLS - A_TAIL
B_TAIL = FOX_REAL - B_HEAD
G_HEAD = SHARD_COLS - B_TAIL
RMS_EPS = 1e-6
LNX_EPS = 64e-5
ATT_SCALE = HEAD ** -0.5
NEG = -1e30

ADAM_LR = 0.001
ADAM_B1 = 0.9
ADAM_B2 = 0.999
ADAM_EPS = 1e-08
ADAM_WD = 0.01
ADAM_STEP = 10

LANES = 128
SUBLANES = 8
VMEM_LIMIT = 56 * 1024 * 1024
MESH = pl.DeviceIdType.MESH


def _params(*sem):
    return pltpu.CompilerParams(dimension_semantics=sem if sem else None, vmem_limit_bytes=VMEM_LIMIT)


def _sigmoid(x):
    return 1.0 / (1.0 + jnp.exp(-x))


def _log_sigmoid(x):
    return jnp.minimum(x, 0.0) - jnp.log(1.0 + jnp.exp(-jnp.abs(x)))


def _head_ones():
    r = lax.broadcasted_iota(jnp.int32, (LANES, LANES), 0) >> 6
    c = lax.broadcasted_iota(jnp.int32, (LANES, LANES), 1) >> 6
    return (r == c).astype(BF16)


def _split3(x):
    hi = x.astype(BF16)
    r1 = x - hi.astype(F32)
    mid = r1.astype(BF16)
    lo = (r1 - mid.astype(F32)).astype(BF16)
    return hi, mid, lo


def _exact_dot(x, ones_bf16, ones_first=False):
    out = None
    for piece in _split3(x):
        if ones_first:
            t = jnp.dot(ones_bf16, piece, preferred_element_type=F32)
        else:
            t = jnp.dot(piece, ones_bf16, preferred_element_type=F32)
        out = t if out is None else out + t
    return out


def _head_sum(x, bd):
    n = x.shape[1] // LANES
    parts = [_exact_dot(x[:, i * LANES:(i + 1) * LANES], bd) for i in range(n)]
    return parts[0] if n == 1 else jnp.concatenate(parts, axis=1)


def _dot_nt(a, b):
    return lax.dot_general(a, b, (((1,), (1,)), ((), ())), preferred_element_type=F32)


def _dot_tn(a, b):
    return lax.dot_general(a, b, (((0,), (0,)), ((), ())), preferred_element_type=F32)


def _colsum(x):
    return jnp.sum(x, axis=0, keepdims=True)


def _matmul_tn_acc(at, b, name, tk=512):
    m, k = at.shape
    n = b.shape[1]

    def body(a_ref, b_ref, o_ref):
        j = pl.program_id(0)

        @pl.when(j == 0)
        def _():
            o_ref[...] = jnp.zeros_like(o_ref)

        o_ref[...] += jnp.dot(a_ref[...], b_ref[...].astype(BF16), preferred_element_type=F32)

    return pl.pallas_call(
        body, name=name, grid=(k // tk,),
        in_specs=[pl.BlockSpec((m, tk), lambda j: (0, j)), pl.BlockSpec((tk, n), lambda j: (j, 0))],
        out_specs=pl.BlockSpec((m, n), lambda j: (0, 0)),
        out_shape=jax.ShapeDtypeStruct((m, n), F32), compiler_params=_params("arbitrary"),
    )(at, b)


def _inproj_bwd(du_a, du_b, du_g, w_a, w_b, w_g, x, dx2, g, exchange=None, tm=256):
    s, d = x.shape
    nb = s // tm

    def body(*refs):
        ((da_ref, db_ref, dg_ref, wa_ref, wb_ref, wg_ref, x_ref, dx2_ref, g_ref), (gx_ref, gg_ref), _,
         moves) = _split_refs(refs, 9, 2, exchange)
        i = pl.program_id(0)
        if moves:
            moves.start(also=(i == 0))

        @pl.when(i == 0)
        def _():
            gg_ref[...] = jnp.zeros_like(gg_ref)

        dh = _dot_nt(da_ref[...].astype(BF16), wa_ref[...])
        dh += _dot_nt(db_ref[...].astype(BF16), wb_ref[...])
        dh += _dot_nt(dg_ref[...].astype(BF16), wg_ref[...])
        xv = x_ref[...]
        r = lax.rsqrt(jnp.mean(xv * xv, axis=-1, keepdims=True) + RMS_EPS)
        xh = xv * r
        gg_ref[...] += _colsum(dh * xh)
        dxh = dh * g_ref[...]
        gx_ref[...] = dx2_ref[...] + r * (dxh - xh * jnp.mean(dxh * xh, axis=-1, keepdims=True))
        if moves:
            moves.wait(also=(i == nb - 1))

    row = lambda w: pl.BlockSpec((tm, w), lambda i: (i, 0))
    full = lambda a: pl.BlockSpec(a.shape, lambda i: (0, 0))
    ex_in = exchange.operands if exchange else []
    ex_out = exchange.out_shapes if exchange else []
    res = pl.pallas_call(
        body, name="inproj_bwd", grid=(nb,),
        in_specs=[row(SEC), row(SEC), row(GATE_COLS), full(w_a), full(w_b), full(w_g), row(d), row(d), full(g)]
                 + [ANY] * len(ex_in),
        out_specs=[row(d), pl.BlockSpec((1, d), lambda i: (0, 0))] + [ANY] * len(ex_out),
        out_shape=[jax.ShapeDtypeStruct((s, d), F32), jax.ShapeDtypeStruct((1, d), F32)] + ex_out,
        scratch_shapes=exchange.scratch() if exchange else [],
        compiler_params=_params("arbitrary"),
    )(du_a, du_b, du_g, w_a, w_b, w_g, x, dx2, g, *ex_in)
    return res[0], res[1], list(res[2:])


def _rwkv_elementwise(ua, prev_row, first, mu, wl, w0, a0, kkw, kaw, bd):
    tm = ua.shape[0]
    rows = lax.broadcasted_iota(jnp.int32, (tm, 1), 0)
    prev = jnp.where(first, jnp.zeros_like(prev_row), prev_row)
    shifted = jnp.where(rows == 0, prev, pltpu.roll(ua, 1, 0))
    delta = shifted - ua
    us = ua + delta * mu
    r = us[:, 0:512]
    k0 = us[:, 512:1024]
    v = us[:, 1024:1536]
    lo = us[:, 1536:1664]
    gate = us[:, 1664:2176]
    lane = lax.broadcasted_iota(jnp.int32, (1, LANES), 1)
    th = jnp.tanh(lo)
    lin = jnp.where(lane < LORA, th, lo)
    ll = jnp.dot(lin.astype(BF16), wl, preferred_element_type=F32)
    sz = _sigmoid(w0 + ll[:, :512])
    e = sz * math.exp(-0.5)
    dec = jnp.exp(-e)
    a = _sigmoid(a0 + ll[:, 512:])
    kk0 = k0 * kkw
    ss = _head_sum(kk0 * kk0, bd)
    nrm = jnp.maximum(jnp.sqrt(ss), 1e-12)
    kk = kk0 / nrm
    k = k0 * (1.0 + (a - 1.0) * kaw)
    return dict(delta=delta, us=us, r=r, k0=k0, v=v, lo=lo, gate=gate, th=th, lin=lin, sz=sz, e=e, dec=dec,
                a=a, kk0=kk0, ss=ss, nrm=nrm, kk=kk, k=k)


def _rwkv_front(x, g, w_a, mu, wl, w0, a0, kkw, kaw, tm=256):
    s, d = x.shape

    def body(x_ref, g_ref, wa_ref, mu_ref, wl_ref, w0_ref, a0_ref, kkw_ref, kaw_ref,
             h_ref, ua_ref, r_ref, w_ref, k_ref, v_ref, a_ref, b_ref, gate_ref, last_row):
        i = pl.program_id(0)
        xv = x_ref[...]
        h = (xv * lax.rsqrt(jnp.mean(xv * xv, axis=-1, keepdims=True) + RMS_EPS) * g_ref[...]).astype(BF16)
        h_ref[...] = h
        ua = jnp.dot(h, wa_ref[...], preferred_element_type=F32)
        ua_ref[...] = ua

        @pl.when(i == 0)
        def _():
            last_row[...] = jnp.zeros_like(last_row)

        f = _rwkv_elementwise(ua, last_row[...], i == 0, mu_ref[...], wl_ref[...], w0_ref[...],
                              a0_ref[...], kkw_ref[...], kaw_ref[...], _head_ones())
        last_row[...] = ua[tm - 1:tm, :]
        r_ref[...] = f["r"]
        w_ref[...] = f["dec"]
        k_ref[...] = f["k"]
        v_ref[...] = f["v"]
        a_ref[...] = -f["kk"]
        b_ref[...] = f["kk"] * f["a"]
        gate_ref[...] = f["gate"]

    vec = lambda w: pl.BlockSpec((1, w), lambda i: (0, 0))
    row = lambda w: pl.BlockSpec((tm, w), lambda i: (i, 0))
    return pl.pallas_call(
        body, name="rwkv_front", grid=(s // tm,),
        in_specs=[row(d), vec(d), pl.BlockSpec(w_a.shape, lambda i: (0, 0), pipeline_mode=pl.Buffered(1)),
                  vec(SEC), pl.BlockSpec((LANES, 2 * D_HALF), lambda i: (0, 0)),
                  vec(D_HALF), vec(D_HALF), vec(D_HALF), vec(D_HALF)],
        out_specs=[row(d), row(SEC)] + [row(D_HALF)] * 7,
        out_shape=[jax.ShapeDtypeStruct((s, d), BF16), jax.ShapeDtypeStruct((s, SEC), F32)]
                  + [jax.ShapeDtypeStruct((s, D_HALF), F32)] * 7,
        scratch_shapes=[pltpu.VMEM((1, SEC), F32)],
        compiler_params=_params("arbitrary"),
    )(x, g, w_a, mu, wl, w0, a0, kkw, kaw)


SCAN_TB = 128
N_PAIRS = 4


def _pair_sum(x, left):
    s_l = jnp.sum(jnp.where(left, x, 0.0), axis=1, keepdims=True)
    s_r = jnp.sum(jnp.where(left, 0.0, x), axis=1, keepdims=True)
    return jnp.where(left, s_l, s_r)


def _pair_dot(x, row_l, row_r, left):
    s_l = jnp.sum(x * row_l, axis=1, keepdims=True)
    s_r = jnp.sum(x * row_r, axis=1, keepdims=True)
    return jnp.where(left, s_l, s_r)


def _halves(rows8):
    lane = lax.broadcasted_iota(jnp.int32, rows8.shape, 1)
    keep_left = (lane & (LANES - 1)) < HEAD
    return jnp.where(keep_left, rows8, 0.0), jnp.where(keep_left, 0.0, rows8)


def _quad_consts():
    lane = lax.broadcasted_iota(jnp.int32, (HEAD, 2 * LANES), 1)
    rowi = lax.broadcasted_iota(jnp.int32, (HEAD, 2 * LANES), 0)
    diag2 = rowi == (lane & (HEAD - 1))
    r = lax.broadcasted_iota(jnp.int32, (2 * LANES, 2 * LANES), 0) >> 6
    c = lax.broadcasted_iota(jnp.int32, (2 * LANES, 2 * LANES), 1) >> 6
    return diag2, (r == c).astype(BF16)


def _rows_to_columns(x8, diag2, bd2):
    lhs = jnp.concatenate([jnp.where(diag2, x8[i:i + 1], 0.0).astype(BF16) for i in range(SUBLANES)], axis=0)
    return jnp.dot(lhs, bd2, preferred_element_type=F32)


def _diag_rows(qtile, diag2, bd2, sub_row2):
    res = jnp.dot(qtile, bd2, preferred_element_type=F32)
    out = jnp.zeros((SUBLANES, 2 * LANES), F32)
    for i in range(SUBLANES):
        out = jnp.where(sub_row2 == i, _colsum(jnp.where(diag2, res[i * HEAD:(i + 1) * HEAD], 0.0)), out)
    return out


def _store_tile(qbuf, slot, p, i, x):
    qbuf[slot, p // 2, i * HEAD:(i + 1) * HEAD, (p % 2) * LANES:(p % 2 + 1) * LANES] = x.astype(BF16)


def _left_half():
    return lax.broadcasted_iota(jnp.int32, (HEAD, LANES), 1) < HEAD


def _split_refs(refs, n_rows, n_out, exchange):
    n_in = len(exchange.operands) if exchange else 0
    n_ex_out = len(exchange.out_shapes) if exchange else 0
    refs = list(refs)
    rows, refs = refs[:n_rows], refs[n_rows:]
    ex_in, refs = refs[:n_in], refs[n_in:]
    outs, refs = refs[:n_out], refs[n_out:]
    ex_out, refs = refs[:n_ex_out], refs[n_ex_out:]
    scratch, sems = (refs[:-3], refs[-3:]) if exchange else (refs, None)
    moves = exchange.moves(ex_in, ex_out, sems) if exchange else None
    return rows, outs, scratch, moves


def _wkv_fwd(r, w, k, a, b, v, exchange=None):
    s = r.shape[0]
    tb = SCAN_TB
    nb = s // tb

    def body(*refs):
        (r_ref, w_ref, k_ref, a_ref, b_ref, v_ref), (y_ref, st_ref), (state, vbuf, qbuf), moves = _split_refs(
            refs, 6, 2, exchange)
        g = pl.program_id(0)
        if moves:
            moves.start(also=(g == 0))

        @pl.when(g == 0)
        def _():
            state[...] = jnp.zeros_like(state)
            qbuf[...] = jnp.zeros_like(qbuf)

        left = _left_half()
        diag2, bd2 = _quad_consts()
        sub_row2 = lax.broadcasted_iota(jnp.int32, (SUBLANES, 2 * LANES), 0)
        groups = tb // SUBLANES
        quads = [slice(g2 * 2 * LANES, (g2 + 1) * 2 * LANES) for g2 in range(2)]

        def rows_of(q):
            return pl.ds(pl.multiple_of(q * SUBLANES, SUBLANES), SUBLANES)

        def v_tiles(q, slot):
            v8 = v_ref[rows_of(q), :]
            for g2 in range(2):
                vbuf[slot, g2] = _rows_to_columns(v8[:, quads[g2]], diag2, bd2)

        def chain(q, slot):
            rows8 = rows_of(q)
            a8, w8, b8, k8, r8 = (x[rows8, :] for x in (a_ref, w_ref, b_ref, k_ref, r_ref))
            pairs = [slice(p * LANES, (p + 1) * LANES) for p in range(N_PAIRS)]
            a_next = pltpu.roll(a8, SUBLANES - 1, 0)
            (a8_l, a8_r), (wa8_l, wa8_r) = _halves(a8), _halves(w8 * a_next)
            ba8 =jnp.concatenate([_pair_sum(b8[:, pr] * a_next[:, pr], left[0:SUBLANES]) for pr in pairs], axis=1)
            ka8 = jnp.concatenate([_pair_sum(k8[:, pr] * a_next[:, pr], left[0:SUBLANES]) for pr in pairs], axis=1)
            sp = [state[p] for p in range(N_PAIRS)]
            for i in range(0, SUBLANES, 2):
                r0, r1 = slice(i, i + 1), slice(i + 1, i + 2)
                sums = [(_pair_dot(sp[p], a8_l[r0, pairs[p]], a8_r[r0, pairs[p]], left),
                         _pair_dot(sp[p], wa8_l[r0, pairs[p]], wa8_r[r0, pairs[p]], left)) for p in range(N_PAIRS)]
                sa0, sa1 = [s[0] for s in sums], [s[1] for s in sums]
                for p in range(N_PAIRS):
                    pr = pairs[p]
                    inner = slice((p % 2) * LANES, (p % 2 + 1) * LANES)
                    vt0 = vbuf[slot, p // 2, i * HEAD:(i + 1) * HEAD, inner]
                    vt1 = vbuf[slot, p // 2, (i + 1) * HEAD:(i + 2) * HEAD, inner]
                    sa_next = sa1[p] + sa0[p] * ba8[r0, pr] + vt0 * ka8[r0, pr]
                    s1 = sp[p] * w8[r0, pr] + sa0[p] * b8[r0, pr] + vt0 * k8[r0, pr]
                    st_ref[q * SUBLANES + i, p] = s1
                    _store_tile(qbuf, slot, p, i, s1 * r8[r0, pr])
                    s2 = s1 * w8[r1, pr] + sa_next * b8[r1, pr] + vt1 * k8[r1, pr]
                    st_ref[q * SUBLANES + i + 1, p] = s2
                    _store_tile(qbuf, slot, p, i + 1, s2 * r8[r1, pr])
                    sp[p] = s2
            for p in range(N_PAIRS):
                state[p] = sp[p]

        def y_rows(q, slot):
            for g2 in range(2):
                y_ref[rows_of(q), quads[g2]] = _diag_rows(qbuf[slot, g2], diag2, bd2, sub_row2)

        v_tiles(0, 0)

        def two_groups(j, carry):
            q0 = 2 * j
            v_tiles(q0 + 1, 1)
            chain(q0, 0)
            y_rows(jnp.maximum(q0 - 1, 0), 1)
            v_tiles(jnp.minimum(q0 + 2, groups - 1), 0)
            chain(q0 + 1, 1)
            y_rows(q0, 0)
            return carry

        lax.fori_loop(0, groups // 2, two_groups, 0)
        y_rows(groups - 1, 1)
        if moves:
            moves.wait(also=(g == nb - 1))

    rows = pl.BlockSpec((tb, D_HALF), lambda g: (g, 0))
    ex_in = exchange.operands if exchange else []
    ex_out = exchange.out_shapes if exchange else []
    res = pl.pallas_call(
        body, name="wkv_fwd", grid=(nb,),
        in_specs=[rows] * 6 + [ANY] * len(ex_in),
        out_specs=[rows, pl.BlockSpec((tb, N_PAIRS, HEAD, LANES), lambda g: (g, 0, 0, 0))] + [ANY] * len(ex_out),
        out_shape=[jax.ShapeDtypeStruct((s, D_HALF), F32),
                   jax.ShapeDtypeStruct((s, N_PAIRS, HEAD, LANES), F32)] + ex_out,
        scratch_shapes=[pltpu.VMEM((N_PAIRS, HEAD, LANES), F32),
                        pltpu.VMEM((2, 2, SUBLANES * HEAD, 2 * LANES), F32),
                        pltpu.VMEM((2, 2, SUBLANES * HEAD, 2 * LANES), BF16)]
                       + (exchange.scratch() if exchange else []),
        compiler_params=_params("arbitrary"),
    )(r, w, k, a, b, v, *ex_in)
    return res[0], res[1], list(res[2:])


def _wkv_bwd(r, w, k, a, b, v, dy, st, exchange=None):
    s = r.shape[0]
    tb = SCAN_TB
    nb = s // tb

    def body(*refs):
        ((r_ref, w_ref, k_ref, a_ref, b_ref, v_ref, dy_ref, st_ref, before_ref),
         (dr_ref, dw_ref, dk_ref, dv_ref, da_ref, db_ref), (dstate, vbuf, qbuf, sbuf),
         moves) = _split_refs(refs, 9, 6, exchange)
        g = pl.program_id(0)
        first_block = g == nb - 1
        if moves:
            moves.start(also=(g == 0))

        @pl.when(g == 0)
        def _():
            dstate[...] = jnp.zeros_like(dstate)
            qbuf[...] = jnp.zeros_like(qbuf)

        left = _left_half()
        diag2, bd2 = _quad_consts()
        sub_row = lax.broadcasted_iota(jnp.int32, (SUBLANES, LANES), 0)
        sub_row2 = lax.broadcasted_iota(jnp.int32, (SUBLANES, 2 * LANES), 0)
        groups = tb // SUBLANES
        quads = [slice(g2 * 2 * LANES, (g2 + 1) * 2 * LANES) for g2 in range(2)]
        row_refs = (dr_ref, dw_ref, dk_ref, da_ref, db_ref)

        def rows_of(q):
            return pl.ds(pl.multiple_of(q * SUBLANES, SUBLANES), SUBLANES)

        def state_before(q, i, p):
            if i > 0:
                return st_ref[q * SUBLANES + i - 1, p]
            return jnp.where(q == 0, jnp.where(first_block, 0.0, before_ref[0, p]),
                             st_ref[jnp.maximum(q * SUBLANES - 1, 0), p])

        def column_tiles(q, slot):
            rows8 = rows_of(q)
            for kind, ref in enumerate((v_ref, dy_ref)):
                x8 = ref[rows8, :]
                for g2 in range(2):
                    vbuf[slot, kind, g2] = _rows_to_columns(x8[:, quads[g2]], diag2, bd2)
            a8 = a_ref[rows8, :]
            for i in range(SUBLANES):
                for p in range(N_PAIRS):
                    _store_tile(sbuf, 0, p, i, state_before(q, i, p) * a8[i:i + 1, p * LANES:(p + 1) * LANES])
            for g2 in range(2):
                vbuf[slot, 2, g2] = jnp.dot(sbuf[0, g2], bd2, preferred_element_type=F32)

        def chain(q, slot):
            rows8 = rows_of(q)
            a8, w8, b8, k8, r8 = (x[rows8, :] for x in (a_ref, w_ref, b_ref, k_ref, r_ref))
            b8_l, b8_r = _halves(b8)
            dsp = [dstate[p] for p in range(N_PAIRS)]
            outs = [[jnp.zeros((SUBLANES, LANES), F32) for _ in row_refs] for _ in range(N_PAIRS)]
            after = [st_ref[q * SUBLANES + SUBLANES - 1, p] for p in range(N_PAIRS)]
            for i in reversed(range(SUBLANES)):
                row = slice(i, i + 1)
                pl_ = [slice(p * LANES, (p + 1) * LANES) for p in range(N_PAIRS)]
                tile = [(p // 2, slice(i * HEAD, (i + 1) * HEAD), slice((p % 2) * LANES, (p % 2 + 1) * LANES))
                        for p in range(N_PAIRS)]
                sp = [state_before(q, i, p) for p in range(N_PAIRS)]
                dyt = [vbuf[(slot, 1) + tile[p]] for p in range(N_PAIRS)]
                ds = [dsp[p] + dyt[p] * r8[row, pl_[p]] for p in range(N_PAIRS)]
                dsa = [_pair_dot(ds[p], b8_l[row, pl_[p]], b8_r[row, pl_[p]], left) for p in range(N_PAIRS)]
                sa = [vbuf[(slot, 2) + tile[p]] for p in range(N_PAIRS)]
                for p in range(N_PAIRS):
                    ar, wr, br, kr = (x[row, pl_[p]] for x in (a8, w8, b8, k8))
                    vt = vbuf[(slot, 0) + tile[p]]
                    dsp[p] = ds[p] * wr + dsa[p] * ar
                    new = (_colsum(after[p] * dyt[p]), _colsum(ds[p] * sp[p]), _colsum(ds[p] * vt),
                           _colsum(sp[p] * dsa[p]), _colsum(ds[p] * sa[p]))
                    outs[p] = [jnp.where(sub_row == i, n, o) for n, o in zip(new, outs[p])]
                    _store_tile(qbuf, slot, p, i, ds[p] * kr)
                after = sp
            for p in range(N_PAIRS):
                dstate[p] = dsp[p]
                for ref, o in zip(row_refs, outs[p]):
                    ref[rows8, p * LANES:(p + 1) * LANES] = o

        def dv_rows(q, slot):
            for g2 in range(2):
                dv_ref[rows_of(q), quads[g2]] = _diag_rows(qbuf[slot, g2], diag2, bd2, sub_row2)

        column_tiles(groups - 1, 0)

        def two_groups(j, carry):
            q0 = groups - 1 - 2 * j
            column_tiles(q0 - 1, 1)
            chain(q0, 0)
            dv_rows(jnp.minimum(q0 + 1, groups - 1), 1)
            column_tiles(jnp.maximum(q0 - 2, 0), 0)
            chain(q0 - 1, 1)
            dv_rows(q0, 0)
            return carry

        lax.fori_loop(0, groups // 2, two_groups, 0)
        dv_rows(0, 1)
        if moves:
            moves.wait(also=(g == nb - 1))

    rows = pl.BlockSpec((tb, D_HALF), lambda g: (nb - 1 - g, 0))
    ex_in = exchange.operands if exchange else []
    ex_out = exchange.out_shapes if exchange else []
    res = pl.pallas_call(
        body, name="wkv_bwd", grid=(nb,),
        in_specs=[rows] * 7 + [pl.BlockSpec((tb, N_PAIRS, HEAD, LANES), lambda g: (nb - 1 - g, 0, 0, 0)),
                               pl.BlockSpec((1, N_PAIRS, HEAD, LANES),
                                            lambda g: (jnp.maximum((nb - 1 - g) * tb - 1, 0), 0, 0, 0))]
                 + [ANY] * len(ex_in),
        out_specs=[rows] * 6 + [ANY] * len(ex_out),
        out_shape=[jax.ShapeDtypeStruct((s, D_HALF), F32)] * 6 + ex_out,
        scratch_shapes=[pltpu.VMEM((N_PAIRS, HEAD, LANES), F32),
                        pltpu.VMEM((2, 3, 2, SUBLANES * HEAD, 2 * LANES), F32),
                        pltpu.VMEM((2, 2, SUBLANES * HEAD, 2 * LANES), BF16),
                        pltpu.VMEM((1, 2, SUBLANES * HEAD, 2 * LANES), BF16)]
                       + (exchange.scratch() if exchange else []),
        compiler_params=_params("arbitrary"),
    )(r, w, k, a, b, v, dy, st, st, *ex_in)
    return list(res[:6]), list(res[6:])


def _rwkv_post_math(y, r, k, v, gate, lw, lb, rk, bd):
    mean = _head_sum(y, bd) * (1.0 / HEAD)
    yc = y - mean
    var = _head_sum(yc * yc, bd) * (1.0 / HEAD)
    rstd = lax.rsqrt(var + LNX_EPS)
    yn = yc * rstd
    rkk = _head_sum(r * k * rk, bd)
    sg = _sigmoid(gate)
    pre = yn * lw + lb + rkk * v
    return yn, rstd, rkk, sg, pre


def _rwkv_prep_bwd(u_a, h_t, grads, mu, wl, w0, a0, kkw, kaw, tm=256):
    s = u_a.shape[0]
    nb = s // tm
    d = h_t.shape[0]

    def body(ua_ref, prev_ref, ht_ref, drs_ref, dws_ref, dks_ref, dvs_ref, das_ref, dbs_ref, drb_ref, dkb_ref, dvb_ref,
             dgt_ref, mu_ref, wl_ref, w0_ref, a0_ref, kkw_ref, kaw_ref,
             du_ref, dwa_ref, dmu_ref, dwl_ref, dw0_ref, da0_ref, dkkw_ref, dkaw_ref, carry):
        i = pl.program_id(0)

        @pl.when(i == 0)
        def _():
            carry[...] = jnp.zeros_like(carry)
            for ref in (dwa_ref, dmu_ref, dwl_ref, dw0_ref, da0_ref, dkkw_ref, dkaw_ref):
                ref[...] = jnp.zeros_like(ref)

        bd = _head_ones()
        mu_v, wl_v, kkw_v, kaw_v = mu_ref[...], wl_ref[...], kkw_ref[...], kaw_ref[...]
        f = _rwkv_elementwise(ua_ref[...], prev_ref[7:8, :], i == nb - 1, mu_v, wl_v, w0_ref[...],
                              a0_ref[...], kkw_v, kaw_v, bd)
        a, kk, k0 = f["a"], f["kk"], f["k0"]
        dk = dks_ref[...] + dkb_ref[...]
        dbs = dbs_ref[...]
        dkk = dbs * a - das_ref[...]
        da = dbs * kk + dk * k0 * kaw_v
        dk0 = dk * (1.0 + (a - 1.0) * kaw_v)
        dkaw_ref[...] += _colsum(dk * k0 * (a - 1.0))
        inv = 1.0 / f["nrm"]
        proj = _head_sum(dkk * kk, bd)
        dkk0 = jnp.where(f["ss"] > 1e-24, (dkk - kk * proj) * inv, dkk * inv)
        dk0 = dk0 + dkk0 * kkw_v
        dkkw_ref[...] += _colsum(dkk0 * k0)
        dza = da * a * (1.0 - a)
        da0_ref[...] += _colsum(dza)
        dz = -dws_ref[...] * f["dec"] * f["e"] * (1.0 - f["sz"])
        dw0_ref[...] += _colsum(dz)
        dll = jnp.concatenate([dz, dza], axis=1).astype(BF16)
        dwl_ref[...] += _dot_tn(f["lin"].astype(BF16), dll)
        dlin = _dot_nt(dll, wl_v)
        lane = lax.broadcasted_iota(jnp.int32, (1, LANES), 1)
        th = f["th"]
        dlo = jnp.where(lane < LORA, dlin * (1.0 - th * th), dlin)
        dus = jnp.concatenate([drs_ref[...] + drb_ref[...], dk0, dvs_ref[...] + dvb_ref[...], dlo, dgt_ref[...]],
                              axis=1)
        dmu_ref[...] += _colsum(dus * f["delta"])
        g1 = dus * mu_v
        rows = lax.broadcasted_iota(jnp.int32, (tm, 1), 0)
        up = jnp.where(rows == tm - 1, carry[...], pltpu.roll(g1, tm - 1, 0))
        dua = dus - g1 + up
        du_ref[...] = dua
        dwa_ref[...] += jnp.dot(ht_ref[...], dua.astype(BF16), preferred_element_type=F32)
        carry[...] = g1[0:1, :]

    rev = lambda w: pl.BlockSpec((tm, w), lambda i: (nb - 1 - i, 0))
    vec = lambda w: pl.BlockSpec((1, w), lambda i: (0, 0))
    wl_spec = pl.BlockSpec((LANES, 2 * D_HALF), lambda i: (0, 0))
    return pl.pallas_call(
        body, name="rwkv_prep_bwd", grid=(nb,),
        in_specs=[rev(SEC), pl.BlockSpec((8, SEC), lambda i: (jnp.maximum((nb - 1 - i) * (tm // 8) - 1, 0), 0)),
                  pl.BlockSpec((d, tm), lambda i: (0, nb - 1 - i))]
                 + [rev(D_HALF)] * 10 + [vec(SEC), wl_spec] + [vec(D_HALF)] * 4,
        out_specs=[rev(SEC), pl.BlockSpec((d, SEC), lambda i: (0, 0)), vec(SEC), wl_spec] + [vec(D_HALF)] * 4,
        out_shape=[jax.ShapeDtypeStruct((s, SEC), F32), jax.ShapeDtypeStruct((d, SEC), F32),
                   jax.ShapeDtypeStruct((1, SEC), F32),
                   jax.ShapeDtypeStruct((LANES, 2 * D_HALF), F32)] + [jax.ShapeDtypeStruct((1, D_HALF), F32)] * 4,
        scratch_shapes=[pltpu.VMEM((1, SEC), F32)],
        compiler_params=_params("arbitrary"),
    )(u_a, u_a, h_t, *grads, mu, wl, w0, a0, kkw, kaw)


def _tri(tm, lower):
    r = lax.broadcasted_iota(jnp.int32, (tm, tm), 0)
    c = lax.broadcasted_iota(jnp.int32, (tm, tm), 1)
    return ((r >= c) if lower else (r <= c)).astype(BF16)


def _head_rms(x, g, bd):
    rinv = lax.rsqrt(_head_sum(x * x, bd) * (1.0 / HEAD) + RMS_EPS)
    xh = x * rinv
    return xh, rinv, xh * g


def _fox_front(h, w_b, fb, qg, kg, tm=256):
    s, d = h.shape

    def body(h_ref, wb_ref, fb_ref, qg_ref, kg_ref, ub_ref, q_ref, k_ref, v_ref, cc_ref, cr_ref, carry):
        i = pl.program_id(0)

        @pl.when(i == 0)
        def _():
            carry[...] = jnp.zeros_like(carry)

        ub_ref[...] = jnp.dot(h_ref[...], wb_ref[...], preferred_element_type=F32)
        bd = _head_ones()
        _, _, qn = _head_rms(ub_ref[:, 0:512], qg_ref[...], bd)
        _, _, kn = _head_rms(ub_ref[:, 512:1024], kg_ref[...], bd)
        q_ref[...] = (qn * ATT_SCALE).astype(BF16)
        k_ref[...] = kn.astype(BF16)
        v_ref[...] = ub_ref[:, 1024:1536].astype(BF16)
        lane = lax.broadcasted_iota(jnp.int32, (1, LANES), 1)
        logf = jnp.where(lane < N_HEADS, _log_sigmoid(ub_ref[:, 2048:2176] + fb_ref[...]), 0.0)
        cum = _exact_dot(logf, _tri(tm, True), ones_first=True) + carry[...]
        for h in range(N_HEADS):
            cc_ref[h] = jnp.broadcast_to(cum[:, h:h + 1], (tm, LANES))
        cr_ref[...] = jnp.transpose(cum)[0:N_HEADS, :]
        carry[...] = cum[tm - 1:tm, :]

    blk = pl.BlockSpec((tm, D_HALF), lambda i: (i, 0))
    return pl.pallas_call(
        body, name="fox_front", grid=(s // tm,),
        in_specs=[pl.BlockSpec((tm, d), lambda i: (i, 0)),
                  pl.BlockSpec(w_b.shape, lambda i: (0, 0), pipeline_mode=pl.Buffered(1)),
                  pl.BlockSpec((1, LANES), lambda i: (0, 0)),
                  pl.BlockSpec((1, D_HALF), lambda i: (0, 0)), pl.BlockSpec((1, D_HALF), lambda i: (0, 0))],
        out_specs=[pl.BlockSpec((tm, SEC), lambda i: (i, 0)), blk, blk, blk,
                   pl.BlockSpec((N_HEADS, tm, LANES), lambda i: (0, i, 0)), pl.BlockSpec((N_HEADS, tm), lambda i: (0, i))],
        out_shape=[jax.ShapeDtypeStruct((s, SEC), F32)] + [jax.ShapeDtypeStruct((s, D_HALF), BF16)] * 3
                  + [jax.ShapeDtypeStruct((N_HEADS, s, LANES), F32), jax.ShapeDtypeStruct((N_HEADS, s), F32)],
        scratch_shapes=[pltpu.VMEM((1, LANES), F32)],
        compiler_params=_params("arbitrary"),
    )(h, w_b, fb, qg, kg)


ATT_T = 256


def _tiles(nblk, by_query):
    if by_query:
        pairs = [(i, j) for i in range(nblk) for j in range(i + 1)]
    else:
        pairs = [(i, j) for j in range(nblk) for i in range(j, nblk)]
    return (jnp.asarray([p[0] for p in pairs], jnp.int32), jnp.asarray([p[1] for p in pairs], jnp.int32))


def _attn_fwd(q, k, v, cc, cr):
    s = q.shape[0]
    t = ATT_T
    nblk = s // t

    def body(qi_ref, kj_ref, q_ref, k_ref, v_ref, cc_ref, cr_ref, o_ref, lse_ref, m_sc, l_sc, acc_sc):
        i = qi_ref[pl.program_id(0)]
        j = kj_ref[pl.program_id(0)]

        @pl.when(j == 0)
        def _():
            m_sc[...] = jnp.full_like(m_sc, NEG)
            l_sc[...] = jnp.zeros_like(l_sc)
            acc_sc[...] = jnp.zeros_like(acc_sc)

        def tile(on_diagonal):
            causal = _causal_tile(t) if on_diagonal else None
            left = lax.broadcasted_iota(jnp.int32, (1, LANES), 1) < HEAD
            for p in range(N_PAIRS):
                lanes = slice(p * LANES, (p + 1) * LANES)
                q2, k2, v2 = q_ref[:, lanes], k_ref[:, lanes], v_ref[:, lanes]
                acc2 = acc_sc[:, lanes]
                for e in range(2):
                    h = 2 * p + e
                    msk = left if e == 0 else jnp.logical_not(left)
                    sc = _dot_nt(jnp.where(msk, q2, jnp.zeros_like(q2)), k2)
                    sc = sc + (_wide(cc_ref[h]) - cr_ref[h:h + 1, :])
                    if on_diagonal:
                        sc = jnp.where(causal, sc, NEG)
                    m_prev = m_sc[h]
                    m_new = jnp.maximum(m_prev, jnp.max(sc, axis=1, keepdims=True))
                    alpha = jnp.exp(m_prev - m_new)
                    pm = jnp.exp(sc - _wide(m_new))
                    l_sc[h] = alpha * l_sc[h] + jnp.sum(pm, axis=1, keepdims=True)
                    m_sc[h] = m_new
                    pv = jnp.dot(pm.astype(BF16), v2, preferred_element_type=F32)
                    acc2 = jnp.where(msk, alpha * acc2 + pv, acc2)
                acc_sc[:, lanes] = acc2

        pl.when(j < i)(functools.partial(tile, False))
        pl.when(j == i)(functools.partial(tile, True))

        @pl.when(j == i)
        def _():
            left = lax.broadcasted_iota(jnp.int32, (1, LANES), 1) < HEAD
            for p in range(N_PAIRS):
                lanes = slice(p * LANES, (p + 1) * LANES)
                inv = jnp.where(left, 1.0 / l_sc[2 * p], 1.0 / l_sc[2 * p + 1])
                o_ref[:, lanes] = acc_sc[:, lanes] * inv
            for h in range(N_HEADS):
                lse_ref[h] = m_sc[h] + jnp.log(l_sc[h])

    qi, kj = _tiles(nblk, by_query=True)
    qblk = pl.BlockSpec((t, D_HALF), lambda n, qi, kj: (qi[n], 0))
    kblk = pl.BlockSpec((t, D_HALF), lambda n, qi, kj: (kj[n], 0))
    qrep = pl.BlockSpec((N_HEADS, t, LANES), lambda n, qi, kj: (0, qi[n], 0))
    return pl.pallas_call(
        body, name="fox_attn_fwd",
        grid_spec=pltpu.PrefetchScalarGridSpec(
            num_scalar_prefetch=2, grid=(qi.shape[0],),
            in_specs=[qblk, kblk, kblk, qrep, pl.BlockSpec((N_HEADS, t), lambda n, qi, kj: (0, kj[n]))],
            out_specs=[qblk, qrep],
            scratch_shapes=[pltpu.VMEM((N_HEADS, t, LANES), F32), pltpu.VMEM((N_HEADS, t, LANES), F32),
                            pltpu.VMEM((t, D_HALF), F32)]),
        out_shape=[jax.ShapeDtypeStruct((s, D_HALF), F32), jax.ShapeDtypeStruct((N_HEADS, s, LANES), F32)],
        compiler_params=_params("arbitrary"),
    )(qi, kj, q, k, v, cc, cr)


def _causal_tile(t):
    return lax.broadcasted_iota(jnp.int32, (t, t), 0) >= lax.broadcasted_iota(jnp.int32, (t, t), 1)


def _wide(x):
    return jnp.concatenate([x, x], axis=1)


def _attn_probs(q2, k2, v2, do2, msk, causal, bias, lse_rows):
    zero = jnp.zeros_like(q2)
    qh = jnp.where(msk, q2, zero)
    doh = jnp.where(msk, do2, zero)
    sc = _dot_nt(qh, k2) + bias
    if causal is not None:
        sc = jnp.where(causal, sc, NEG)
    pm = jnp.exp(sc - _wide(lse_rows))
    dp = _dot_nt(doh, v2)
    return qh, doh, pm, dp


def _attn_bwd_rowdot(q, k, v, do, lse, cc, cr):
    s = q.shape[0]
    t = ATT_T
    nblk = s // t

    def body(qi_ref, kj_ref, q_ref, k_ref, v_ref, do_ref, lse_ref, cc_ref, cr_ref, dd_ref, acc):
        i = qi_ref[pl.program_id(0)]
        j = kj_ref[pl.program_id(0)]

        @pl.when(j == 0)
        def _():
            acc[...] = jnp.zeros_like(acc)

        def tile(on_diagonal):
            causal = _causal_tile(t) if on_diagonal else None
            left = lax.broadcasted_iota(jnp.int32, (1, LANES), 1) < HEAD
            for p in range(N_PAIRS):
                lanes = slice(p * LANES, (p + 1) * LANES)
                q2, k2, v2, do2 = q_ref[:, lanes], k_ref[:, lanes], v_ref[:, lanes], do_ref[:, lanes]
                for e in range(2):
                    h = 2 * p + e
                    msk = left if e == 0 else jnp.logical_not(left)
                    bias = _wide(cc_ref[h]) - cr_ref[h:h + 1, :]
                    _, _, pm, dp = _attn_probs(q2, k2, v2, do2, msk, causal, bias, lse_ref[h])
                    acc[h] += jnp.sum(pm * dp, axis=1, keepdims=True)

        pl.when(j < i)(functools.partial(tile, False))
        pl.when(j == i)(functools.partial(tile, True))

        @pl.when(j == i)
        def _():
            dd_ref[...] = acc[...]

    qi, kj = _tiles(nblk, by_query=True)
    qblk = pl.BlockSpec((t, D_HALF), lambda n, qi, kj: (qi[n], 0))
    qcol = pl.BlockSpec((N_HEADS, t, LANES), lambda n, qi, kj: (0, qi[n], 0))
    kblk = pl.BlockSpec((t, D_HALF), lambda n, qi, kj: (kj[n], 0))
    return pl.pallas_call(
        body, name="fox_attn_rowdot",
        grid_spec=pltpu.PrefetchScalarGridSpec(
            num_scalar_prefetch=2, grid=(qi.shape[0],),
            in_specs=[qblk, kblk, kblk, qblk, qcol, qcol, pl.BlockSpec((N_HEADS, t), lambda n, qi, kj: (0, kj[n]))],
            out_specs=qcol, scratch_shapes=[pltpu.VMEM((N_HEADS, t, LANES), F32)]),
        out_shape=jax.ShapeDtypeStruct((N_HEADS, s, LANES), F32),
        compiler_params=_params("arbitrary"),
    )(qi, kj, q, k, v, do, lse, cc, cr)


def _attn_bwd(q, k, v, do, lse, dd, cc, cr):
    s = q.shape[0]
    t = ATT_T
    nblk = s // t

    def body(qi_ref, kj_ref, q_ref, k_ref, v_ref, do_ref, lse_ref, dd_ref, cc_ref, cr_ref,
             dq_ref, dk_ref, dv_ref, dcr_ref, dk_sc, dv_sc, dcr_sc):
        i = qi_ref[pl.program_id(0)]
        j = kj_ref[pl.program_id(0)]

        @pl.when(pl.program_id(0) == 0)
        def _():
            dq_ref[...] = jnp.zeros_like(dq_ref)

        @pl.when(i == j)
        def _():
            dk_sc[...] = jnp.zeros_like(dk_sc)
            dv_sc[...] = jnp.zeros_like(dv_sc)
            dcr_sc[...] = jnp.zeros_like(dcr_sc)

        def tile(on_diagonal):
            causal = _causal_tile(t) if on_diagonal else None
            left = lax.broadcasted_iota(jnp.int32, (1, LANES), 1) < HEAD
            qrows = pl.ds(pl.multiple_of(i * t, t), t)
            for p in range(N_PAIRS):
                lanes = slice(p * LANES, (p + 1) * LANES)
                q2, k2, v2, do2 = q_ref[:, lanes], k_ref[:, lanes], v_ref[:, lanes], do_ref[:, lanes]
                zero = jnp.zeros_like(q2)
                dq2 = jnp.zeros((t, LANES), F32)
                dk2 = jnp.zeros((t, LANES), F32)
                dv2 = jnp.zeros((t, LANES), F32)
                for e in range(2):
                    h = 2 * p + e
                    msk = left if e == 0 else jnp.logical_not(left)
                    bias = _wide(cc_ref[h]) - cr_ref[h:h + 1, :]
                    qh, doh, pm, dp = _attn_probs(q2, k2, v2, do2, msk, causal, bias, lse_ref[h])
                    dsc = pm * (dp - _wide(dd_ref[h]))
                    dsb = dsc.astype(BF16)
                    dv2 += _dot_tn(pm.astype(BF16), doh)
                    dk2 += _dot_tn(dsb, qh)
                    dq2 += jnp.dot(dsb, jnp.where(msk, k2, zero), preferred_element_type=F32)
                    dcr_sc[h:h + 1, :] += -_colsum(dsc)
                dq_ref[qrows, lanes] += dq2 * ATT_SCALE
                dk_sc[:, lanes] += dk2
                dv_sc[:, lanes] += dv2

        pl.when(i > j)(functools.partial(tile, False))
        pl.when(i == j)(functools.partial(tile, True))

        @pl.when(i == nblk - 1)
        def _():
            dk_ref[...] = dk_sc[...]
            dv_ref[...] = dv_sc[...]
            dcr_ref[...] = dcr_sc[...]

    qi, kj = _tiles(nblk, by_query=False)
    qblk = pl.BlockSpec((t, D_HALF), lambda n, qi, kj: (qi[n], 0))
    qcol = pl.BlockSpec((N_HEADS, t, LANES), lambda n, qi, kj: (0, qi[n], 0))
    kblk = pl.BlockSpec((t, D_HALF), lambda n, qi, kj: (kj[n], 0))
    krow = pl.BlockSpec((N_HEADS, t), lambda n, qi, kj: (0, kj[n]))
    return pl.pallas_call(
        body, name="fox_attn_bwd",
        grid_spec=pltpu.PrefetchScalarGridSpec(
            num_scalar_prefetch=2, grid=(qi.shape[0],),
            in_specs=[qblk, kblk, kblk, qblk, qcol, qcol, qcol, krow],
            out_specs=[pl.BlockSpec((s, D_HALF), lambda n, qi, kj: (0, 0)), kblk, kblk, krow],
            scratch_shapes=[pltpu.VMEM((t, D_HALF), F32), pltpu.VMEM((t, D_HALF), F32), pltpu.VMEM((N_HEADS, t), F32)]),
        out_shape=[jax.ShapeDtypeStruct((s, D_HALF), F32)] * 3 + [jax.ShapeDtypeStruct((N_HEADS, s), F32)],
        compiler_params=_params("arbitrary"),
    )(qi, kj, q, k, v, do, lse, dd, cc, cr)


def _fox_prep_bwd(u_b, h_t, dq, dk, dv, dgate, dcum, fb, qg, kg, tm=256):
    s = u_b.shape[0]
    nb = s // tm
    d = h_t.shape[0]

    def body(ub_ref, ht_ref, dq_ref, dk_ref, dv_ref, dg_ref, dc_ref, fb_ref, qg_ref, kg_ref,
             du_ref, dwb_ref, dqg_ref, dkg_ref, dfb_ref, carry):
        i = pl.program_id(0)

        @pl.when(i == 0)
        def _():
            carry[...] = jnp.zeros_like(carry)
            dwb_ref[...] = jnp.zeros_like(dwb_ref)
            dqg_ref[...] = jnp.zeros_like(dqg_ref)
            dkg_ref[...] = jnp.zeros_like(dkg_ref)
            dfb_ref[...] = jnp.zeros_like(dfb_ref)

        bd = _head_ones()
        for lo, g_ref, d_ref, dgain_ref in ((0, qg_ref, dq_ref, dqg_ref), (512, kg_ref, dk_ref, dkg_ref)):
            gain = g_ref[...]
            xh, rinv, _ = _head_rms(ub_ref[:, lo:lo + 512], gain, bd)
            dn = d_ref[...]
            dgain_ref[...] += _colsum(dn * xh)
            dxh = dn * gain
            du_ref[:, lo:lo + 512] = rinv * (dxh - xh * (_head_sum(dxh * xh, bd) * (1.0 / HEAD)))
        du_ref[:, 1024:1536] = dv_ref[...]
        du_ref[:, 1536:2048] = dg_ref[...]
        lane = lax.broadcasted_iota(jnp.int32, (1, LANES), 1)
        dc = dc_ref[...]
        dlogf = _exact_dot(dc, _tri(tm, False), ones_first=True) + carry[...]
        carry[...] += _colsum(dc)
        fl = ub_ref[:, 2048:2176] + fb_ref[...]
        dfl = jnp.where(lane < N_HEADS, dlogf * (1.0 - _sigmoid(fl)), 0.0)
        du_ref[:, 2048:2176] = dfl
        dfb_ref[...] += _colsum(dfl)
        dwb_ref[...] += jnp.dot(ht_ref[...], du_ref[...].astype(BF16), preferred_element_type=F32)

    rev = lambda w: pl.BlockSpec((tm, w), lambda i: (nb - 1 - i, 0))
    vec = lambda w: pl.BlockSpec((1, w), lambda i: (0, 0))
    return pl.pallas_call(
        body, name="fox_prep_bwd", grid=(nb,),
        in_specs=[rev(SEC), pl.BlockSpec((d, tm), lambda i: (0, nb - 1 - i))] + [rev(D_HALF)] * 4
                 + [rev(LANES), vec(LANES), vec(D_HALF), vec(D_HALF)],
        out_specs=[rev(SEC), pl.BlockSpec((d, SEC), lambda i: (0, 0)), vec(D_HALF), vec(D_HALF), vec(LANES)],
        out_shape=[jax.ShapeDtypeStruct((s, SEC), F32), jax.ShapeDtypeStruct((d, SEC), F32),
                   jax.ShapeDtypeStruct((1, D_HALF), F32), jax.ShapeDtypeStruct((1, D_HALF), F32),
                   jax.ShapeDtypeStruct((1, LANES), F32)],
        scratch_shapes=[pltpu.VMEM((1, LANES), F32)],
        compiler_params=_params("arbitrary"),
    )(u_b, h_t, dq, dk, dv, dgate, dcum, fb, qg, kg)


def _merge(y, r, k, v, gate_a, o, u_b, h, x, tgt, w_g, wa, wb, wo, fg, lw, lb, rk, tm=256):
    s, d = x.shape

    def body(y_ref, r_ref, k_ref, v_ref, ga_ref, o_ref, gb_ref, h_ref, x_ref, t_ref, wg_ref, wa_ref, wb_ref, wo_ref,
             fg_ref, lw_ref, lb_ref, rk_ref,
             dx2_ref, dy_ref, drb_ref, dkb_ref, dvb_ref, dga_ref, do_ref, dgb_ref, dug_ref,
             dwa_ref, dwb_ref, dwo_ref, dfg_ref, loss_ref, dlw_ref, dlb_ref, drk_ref):
        i = pl.program_id(0)

        @pl.when(i == 0)
        def _():
            for ref in (dwa_ref, dwb_ref, dwo_ref, dfg_ref, loss_ref, dlw_ref, dlb_ref, drk_ref):
                ref[...] = jnp.zeros_like(ref)

        bd = _head_ones()
        wa_v, wb_v, wo_v, fg_v = wa_ref[...], wb_ref[...], wo_ref[...], fg_ref[...]
        rv, kv, vv, ga, lw_v, rk_v = r_ref[...], k_ref[...], v_ref[...], ga_ref[...], lw_ref[...], rk_ref[...]
        yn, rstd, rkk, sga, pre = _rwkv_post_math(y_ref[...], rv, kv, vv, ga, lw_v, lb_ref[...], rk_v, bd)
        silu_a = ga * sga
        gb, ov = gb_ref[...], o_ref[...]
        sgb = _sigmoid(gb)
        silu_b = gb * sgb
        ma = (pre * silu_a).astype(BF16)
        mb = (ov * silu_b).astype(BF16)
        ya = jnp.dot(ma, wa_v, preferred_element_type=F32)
        yb = jnp.dot(mb, wb_v, preferred_element_type=F32)
        ug = jnp.dot(h_ref[...], wg_ref[...], preferred_element_type=F32)
        sa = _sigmoid(ug[:, 0:d])
        sb = _sigmoid(ug[:, d:2 * d])
        merged = (sa * ya + sb * yb).astype(BF16)
        x2 = x_ref[...] + jnp.dot(merged, wo_v, preferred_element_type=F32)
        r2 = lax.rsqrt(jnp.mean(x2 * x2, axis=-1, keepdims=True) + RMS_EPS)
        x2h = x2 * r2
        err = x2h * fg_v - t_ref[...]
        loss_ref[...] += _colsum(err * err)
        dyo = err * (1.0 / d)
        dfg_ref[...] += _colsum(dyo * x2h)
        dx2h = dyo * fg_v
        dx2 = r2 * (dx2h - x2h * jnp.mean(dx2h * x2h, axis=-1, keepdims=True))
        dx2_ref[...] = dx2
        dx2b = dx2.astype(BF16)
        dmerged = _dot_nt(dx2b, wo_v)
        dwo_ref[...] += _dot_tn(merged, dx2b)
        dya = dmerged * sa
        dyb = dmerged * sb
        dug_ref[:, 0:d] = dya * ya * (1.0 - sa)
        dug_ref[:, d:2 * d] = dyb * yb * (1.0 - sb)
        dyab = dya.astype(BF16)
        dybb = dyb.astype(BF16)
        dwa_ref[...] += _dot_tn(ma, dyab)
        dwb_ref[...] += _dot_tn(mb, dybb)
        dmb = _dot_nt(dybb, wb_v)
        do_ref[...] = (dmb * silu_b).astype(BF16)
        dgb_ref[...] = dmb * ov * (sgb * (1.0 + gb * (1.0 - sgb)))
        dma = _dot_nt(dyab, wa_v)
        dga_ref[...] = dma * pre * (sga * (1.0 + ga * (1.0 - sga)))
        dpre = dma * silu_a
        dlw_ref[...] += _colsum(dpre * yn)
        dlb_ref[...] += _colsum(dpre)
        dyn = dpre * lw_v
        m1 = _head_sum(dyn, bd) * (1.0 / HEAD)
        m2 = _head_sum(dyn * yn, bd) * (1.0 / HEAD)
        dy_ref[...] = rstd * (dyn - m1 - yn * m2)
        dvb_ref[...] = dpre * rkk
        drkk = _head_sum(dpre * vv, bd)
        drb_ref[...] = drkk * kv * rk_v
        dkb_ref[...] = drkk * rv * rk_v
        drk_ref[...] += _colsum(drkk * rv * kv)

    row = lambda w: pl.BlockSpec((tm, w), lambda i: (i, 0))
    full = lambda a: pl.BlockSpec(a.shape, lambda i: (0, 0))
    once = lambda a: pl.BlockSpec(a.shape, lambda i: (0, 0), pipeline_mode=pl.Buffered(1))
    half = jax.ShapeDtypeStruct((s, D_HALF), F32)
    fshape = lambda a: jax.ShapeDtypeStruct(a.shape, F32)
    return pl.pallas_call(
        body, name="merge_fwd_bwd", grid=(s // tm,),
        in_specs=[row(D_HALF)] * 6 + [pl.BlockSpec((tm, D_HALF), lambda i: (i, 3)), row(d), row(d), row(d),
                                      once(w_g), once(wa), once(wb), once(wo), full(fg), full(lw), full(lb), full(rk)],
        out_specs=[row(d)] + [row(D_HALF)] * 7 + [row(GATE_COLS), full(wa), full(wb), full(wo), full(fg), full(fg),
                                                   full(lw), full(lb), full(rk)],
        out_shape=[jax.ShapeDtypeStruct((s, d), F32)] + [half] * 5 + [jax.ShapeDtypeStruct((s, D_HALF), BF16), half,
                                                                    jax.ShapeDtypeStruct((s, GATE_COLS), F32),
                                                                    fshape(wa), fshape(wb), fshape(wo), fshape(fg),
                                                                    fshape(fg), fshape(lw), fshape(lb), fshape(rk)],
        compiler_params=_params("arbitrary"),
    )(y, r, k, v, gate_a, o, u_b, h, x, tgt, w_g, wa, wb, wo, fg, lw, lb, rk)


def _lora_weight(w_up, a_up):
    z = jnp.zeros((LORA, D_HALF), w_up.dtype)
    return jnp.concatenate([jnp.concatenate([w_up, z], axis=1), jnp.concatenate([z, a_up], axis=1)], axis=0)


def _device_grads(x, tgt, p, w_a, w_up, a_up, late_weights, fwd_exchange=None, bwd_exchange=None, tail_exchange=None):
    wl = _lora_weight(w_up, a_up)
    rk = p["r_k"].reshape(1, D_HALF)
    fb = jnp.pad(p["f_bias"], ((0, 0), (0, LANES - N_HEADS)))
    qg = jnp.tile(p["q_norm_g"], (1, N_HEADS))
    kg = jnp.tile(p["k_norm_g"], (1, N_HEADS))
    fg = p["final_norm_g"].reshape(1, D_MODEL)
    mixer = (p["shift_mu"], wl, p["w0"], p["a0"], p["k_k"], p["k_a"])

    h, u_a, r, dec, k, v, av, bv, gate_a = _rwkv_front(x, p["norm_g"], w_a, *mixer)
    y, st, arrived = _wkv_fwd(r, dec, k, av, bv, v, fwd_exchange)

    w_b, w_g, w_out_a, w_out_b, w_out = late_weights(arrived)
    u_b, q, kn, vb, cc, cr = _fox_front(h, w_b, fb, qg, kg)
    o, lse = _attn_fwd(q, kn, vb, cc, cr)

    (dx2, dy, dr_b, dk_b, dv_b, dgate_a, do, dgate_b, du_g, dwa, dwb, dwo, dfg, loss_vec, dlw, dlb, drk) = _merge(
        y, r, k, v, gate_a, o, u_b, h, x, tgt, w_g, w_out_a, w_out_b, w_out, fg, p["lnx_w"], p["lnx_b"], rk)

    dd = _attn_bwd_rowdot(q, kn, vb, do, lse, cc, cr)
    dq, dk_att, dv_att, dcr = _attn_bwd(q, kn, vb, do, lse, dd, cc, cr)
    dcum = jnp.pad(dcr.T, ((0, 0), (0, LANES - N_HEADS)))
    h_t = h.T
    du_b, dw_b, dqg, dkg, dfb = _fox_prep_bwd(u_b, h_t, dq, dk_att, dv_att, dgate_b, dcum, fb, qg, kg)
    dw_g = _matmul_tn_acc(h_t, du_g, "dw_gate")

    scan_grads, sent = _wkv_bwd(r, dec, k, av, bv, v, dy, st,
                                bwd_exchange(dw_b, dw_g, dwa, dwb, dwo) if bwd_exchange else None)
    du_a, dw_a, dmu, dwl, dw0, da0, dkkw, dkaw = _rwkv_prep_bwd(
        u_a, h_t, (*scan_grads, dr_b, dk_b, dv_b, dgate_a), *mixer)
    dw_up, da_up = dwl[:LORA, :D_HALF], dwl[LORA:, D_HALF:]
    sent_last = _run_on_sequencer(tail_exchange(dw_a, dw_up, da_up), "scatter_tail", 1) if tail_exchange else []
    grad_x, dnorm_g, _ = _inproj_bwd(du_a, du_b, du_g, w_a, w_b, w_g, x, dx2, p["norm_g"])

    grads = dict(
        norm_g=dnorm_g, w_in=(dw_a, dw_b, dw_g), shift_mu=dmu,
        w_lora_up=dw_up, w0=dw0, a_lora_up=da_up, a0=da0, k_k=dkkw, k_a=dkaw,
        r_k=drk.reshape(1, N_HEADS, HEAD), lnx_w=dlw, lnx_b=dlb, f_bias=dfb[:, :N_HEADS],
        q_norm_g=dqg.reshape(N_HEADS, HEAD).sum(axis=0, keepdims=True),
        k_norm_g=dkg.reshape(N_HEADS, HEAD).sum(axis=0, keepdims=True),
        w_out_a=dwa, w_out_b=dwb, w_out=dwo, final_norm_g=dfg.reshape(D_MODEL))
    return loss_vec, grad_x, grads, sent, sent_last


CHIP_FLIPS = ((1, 0), (0, 1), (1, 1))
ANY = pl.BlockSpec(memory_space=pl.ANY)


def _position():
    return lax.axis_index("x"), lax.axis_index("y"), lax.axis_index("c")


def _flip(v, f):
    return 1 - v if f else v


def _both(a, b):
    if a is None:
        return b
    return a if b is None else jnp.logical_and(a, b)


def _when(cond, fn):
    if cond is None:
        fn()
    else:
        pl.when(cond)(fn)


class _Moves:
    def __init__(self, send_sems, recv_sems, local_sems):
        self.send_sems, self.recv_sems, self.local_sems = send_sems, recv_sems, local_sems
        self.remote, self.local = [], []

    def send(self, src, dst, peer, landing, send_if=None, recv_if=None, first=False):
        k = len(self.remote)
        sems = dict(send_sem=self.send_sems.at[k], recv_sem=self.recv_sems.at[k], device_id=peer, device_id_type=MESH)
        out = pltpu.make_async_remote_copy(src_ref=src, dst_ref=dst, **sems)
        arrival = pltpu.make_async_remote_copy(src_ref=src, dst_ref=landing, **sems)
        self.remote.append((out, arrival, send_if, recv_if, first))

    def copy(self, src, dst, cond=None):
        cp = pltpu.make_async_copy(src, dst, self.local_sems.at[len(self.local)])
        self.local.append((cp, cond))

    def start(self, also=None):
        for cp, cond in self.local:
            _when(_both(also, cond), cp.start)
        for out, _, send_if, _, _ in self.remote:
            _when(_both(also, send_if), out.start)

    def wait_arrivals(self, also=None, first=None):
        for _, arrival, _, recv_if, is_first in self.remote:
            if first is None or first == is_first:
                _when(_both(also, recv_if), arrival.wait_recv)

    def wait_sent(self, also=None):
        for out, _, send_if, _, _ in self.remote:
            _when(_both(also, send_if), out.wait_send)
        for cp, cond in self.local:
            _when(_both(also, cond), cp.wait)

    def wait(self, also=None):
        self.wait_arrivals(also)
        self.wait_sent(also)


class _Exchange:
    def __init__(self, operands, out_shapes, n_remote, n_local, build, relays=None, in_place=()):
        self.operands, self.out_shapes = list(operands), list(out_shapes)
        self.n_remote, self.n_local, self.build = n_remote, n_local, build
        self.relays, self.in_place = relays, in_place

    def scratch(self):
        return [pltpu.SemaphoreType.DMA((self.n_remote,)), pltpu.SemaphoreType.DMA((self.n_remote,)),
                pltpu.SemaphoreType.DMA((max(self.n_local, 1),))]

    def moves(self, in_refs, out_refs, sems):
        mv = _Moves(*sems)
        self.build(mv, in_refs, out_refs)
        return mv


def _run_on_sequencer(exchange, name, collective_id):
    ins = [jax.new_ref(a, memory_space=pltpu.MemorySpace.HBM) for a in exchange.operands]
    outs = [ins[i] if i in exchange.in_place else jax.empty_ref(s, memory_space=pltpu.MemorySpace.HBM)
            for i, s in enumerate(exchange.out_shapes)]
    relay_scratch = [pltpu.SemaphoreType.DMA((n,)) for n, _ in exchange.relays or () for _ in range(2)]

    def launch(*sems):
        x, y, c = _position()
        peers = [(_flip(x, fx), _flip(y, fy), c) for fx, fy in CHIP_FLIPS] + ([(x, y, 1 - c)] if exchange.relays else [])
        barrier = pltpu.get_barrier_semaphore()
        for peer in peers:
            pl.semaphore_signal(barrier, inc=1, device_id=peer, device_id_type=MESH)
        pl.semaphore_wait(barrier, len(peers))
        moves = exchange.moves(ins, outs, sems[:3])
        moves.start()
        if exchange.relays:
            (_, forward), (_, to_sibling) = exchange.relays
            onward, passed = _Moves(sems[3], sems[4], None), _Moves(sems[5], sems[6], None)
            forward(onward, ins, outs)
            to_sibling(passed, ins, outs)
            moves.wait_arrivals(first=True)
            onward.start()
            moves.wait_arrivals(first=False)
            onward.wait_arrivals()
            passed.start()
            passed.wait()
            onward.wait_sent()
        else:
            moves.wait_arrivals()
        moves.wait_sent()

    pl.kernel(launch, mesh=plsc.ScalarSubcoreMesh(axis_name="sequencer", num_cores=1), name=name,
              scratch_types=tuple(exchange.scratch() + relay_scratch),
              compiler_params=pltpu.CompilerParams(collective_id=collective_id))()
    return [o[...] for o in outs]


def _row_major_copy(a, name):
    r, c = a.shape
    tr = _row_tile(r)

    def body(a_ref, o_ref):
        o_ref[...] = a_ref[...]

    blk = pl.BlockSpec((tr, c), lambda i: (i, 0))
    return pl.pallas_call(body, name=name, grid=(r // tr,), in_specs=[blk], out_specs=blk,
                          out_shape=jax.ShapeDtypeStruct(a.shape, a.dtype), compiler_params=_params("parallel"))(a)


def _is_chip(x, y, chip):
    return jnp.logical_and(x == chip // 2, y == chip % 2)


def _gather_exchange(from_chip, from_all, split=()):
    n1, n2 = len(from_chip), len(from_all)
    near = CHIP_FLIPS[:2]

    def quarters(t, c, first, count=1):
        n = from_chip[t][1].shape[0] // 4
        return pl.ds((2 * c + first) * n, count * n)

    def build(mv, ins, outs):
        x, y, c = _position()
        me = 2 * x + y
        for t, (chip, _) in enumerate(from_chip):
            if t not in split:
                mv.copy(ins[t], outs[t], cond=_is_chip(x, y, chip))
        for t in range(n2):
            mv.copy(ins[n1 + t], outs[n1 + t].at[me])
        for t in split:
            for first in (True, False):
                for f, (fx, fy) in enumerate(near):
                    px, py = _flip(x, fx), _flip(y, fy)
                    part = quarters(t, c, f if first else 1 - f)
                    mv.send(ins[t].at[part], outs[t].at[part], (px, py, c), landing=outs[t].at[part], first=first,
                            send_if=_is_chip(x, y, from_chip[t][0]), recv_if=_is_chip(px, py, from_chip[t][0]))
        for fx, fy in CHIP_FLIPS:
            px, py = _flip(x, fx), _flip(y, fy)
            peer = (px, py, c)
            for t, (chip, _) in enumerate(from_chip):
                if t not in split:
                    mv.send(ins[t], outs[t], peer, landing=outs[t],
                            send_if=_is_chip(x, y, chip), recv_if=_is_chip(px, py, chip))
            for t in range(n2):
                mv.send(ins[n1 + t], outs[n1 + t].at[me], peer, landing=outs[n1 + t].at[2 * px + py])

    def forward(mv, ins, outs):
        x, y, c = _position()
        for t in split:
            chip = from_chip[t][0]
            for f, (fx, fy) in enumerate(near):
                gx, gy = near[1 - f]
                part = quarters(t, c, f)
                mv.send(outs[t].at[part], outs[t].at[part], (_flip(x, gx), _flip(y, gy), c), landing=outs[t].at[part],
                        send_if=_is_chip(_flip(x, fx), _flip(y, fy), chip), recv_if=_is_chip(1 - x, 1 - y, chip))

    def to_sibling(mv, ins, outs):
        x, y, c = _position()
        for t in split:
            came = jnp.logical_not(_is_chip(x, y, from_chip[t][0]))
            mv.send(outs[t].at[quarters(t, c, 0, 2)], outs[t].at[quarters(t, c, 0, 2)], (x, y, 1 - c),
                    landing=outs[t].at[quarters(t, 1 - c, 0, 2)], send_if=came, recv_if=came)

    arrays = [a for _, a in from_chip] + list(from_all)
    shapes = [jax.ShapeDtypeStruct(a.shape, a.dtype) for _, a in from_chip]
    shapes += [jax.ShapeDtypeStruct((N_CHIPS,) + a.shape, a.dtype) for a in from_all]
    n_remote = len(CHIP_FLIPS) * (n1 - len(split) + n2) + 2 * len(near) * len(split)
    relays = ((len(near) * len(split), forward), (len(split), to_sibling)) if split else None
    return _Exchange(arrays, shapes, n_remote, n1 + n2, build, relays, in_place=split)


def _scatter_exchange(to_chip, to_all):
    n1, n2 = len(to_chip), len(to_all)

    def build(mv, ins, outs):
        x, y, c = _position()
        for f, (fx, fy) in enumerate(CHIP_FLIPS):
            px, py = _flip(x, fx), _flip(y, fy)
            peer = (px, py, c)
            for t, (chip, _) in enumerate(to_chip):
                mv.send(ins[t], outs[t].at[f], peer, landing=outs[t].at[f],
                        send_if=_is_chip(px, py, chip), recv_if=_is_chip(x, y, chip))
            for t in range(n2):
                mv.send(ins[n1 + t].at[2 * px + py], outs[n1 + t].at[f], peer, landing=outs[n1 + t].at[f])

    arrays = [a for _, a in to_chip] + list(to_all)
    shapes = [jax.ShapeDtypeStruct((len(CHIP_FLIPS),) + a.shape, a.dtype) for _, a in to_chip]
    shapes += [jax.ShapeDtypeStruct((len(CHIP_FLIPS),) + a.shape[1:], a.dtype) for a in to_all]
    return _Exchange(arrays, shapes, len(CHIP_FLIPS) * (n1 + n2), 0, build)


def _swap_sibling(tensors, name):
    n = len(tensors)

    def body(*refs):
        ins, outs = refs[:n], refs[n:2 * n]
        send_sems, recv_sems = refs[2 * n:]
        x, y, c = _position()
        copies = [pltpu.make_async_remote_copy(
            src_ref=ins[t], dst_ref=outs[t], send_sem=send_sems.at[t], recv_sem=recv_sems.at[t],
            device_id=(x, y, 1 - c), device_id_type=MESH) for t in range(n)]
        for cp in copies:
            cp.start()
        for cp in copies:
            cp.wait_recv()
        for cp in copies:
            cp.wait_send()

    return pl.pallas_call(
        body, name=name, in_specs=[ANY] * n, out_specs=[ANY] * n,
        out_shape=[jax.ShapeDtypeStruct(a.shape, a.dtype) for a in tensors],
        scratch_shapes=[pltpu.SemaphoreType.DMA((n,)), pltpu.SemaphoreType.DMA((n,))],
        compiler_params=pltpu.CompilerParams(has_side_effects=True),
    )(*tensors)


def _allreduce_small(slab):
    stages = 3

    def body(x_ref, o_ref, buf, send_sems, recv_sems):
        x, y, c = _position()
        peers = ((1 - x, y, c), (x, 1 - y, c), (x, y, 1 - c))
        o_ref[...] = x_ref[...]
        for k, peer in enumerate(peers):
            cp = pltpu.make_async_remote_copy(src_ref=o_ref, dst_ref=buf.at[k], send_sem=send_sems.at[k],
                                              recv_sem=recv_sems.at[k], device_id=peer, device_id_type=MESH)
            cp.start()
            cp.wait()
            o_ref[...] = o_ref[...] + buf[k]

    return pl.pallas_call(
        body, name="allreduce_small",
        in_specs=[pl.BlockSpec(memory_space=pltpu.VMEM)], out_specs=pl.BlockSpec(memory_space=pltpu.VMEM),
        out_shape=jax.ShapeDtypeStruct(slab.shape, slab.dtype),
        scratch_shapes=[pltpu.VMEM((stages,) + slab.shape, slab.dtype),
                        pltpu.SemaphoreType.DMA((stages,)), pltpu.SemaphoreType.DMA((stages,))],
        compiler_params=pltpu.CompilerParams(has_side_effects=True),
    )(slab)


def _row_tile(r):
    return min(r, 256)


def _sum4(stack, recv, me):
    _, r, c = stack.shape
    tr = _row_tile(r)

    def body(me_ref, own_ref, recv_ref, o_ref):
        o_ref[...] = (((own_ref[...] + recv_ref[0].astype(F32)) + recv_ref[1].astype(F32))
                      + recv_ref[2].astype(F32))

    return pl.pallas_call(
        body, name="sum_partials",
        grid_spec=pltpu.PrefetchScalarGridSpec(
            num_scalar_prefetch=1, grid=(r // tr,),
            in_specs=[pl.BlockSpec((None, tr, c), lambda i, me_ref: (me_ref[0], i, 0)),
                      pl.BlockSpec((len(CHIP_FLIPS), tr, c), lambda i, me_ref: (0, i, 0))],
            out_specs=pl.BlockSpec((tr, c), lambda i, me_ref: (i, 0))),
        out_shape=jax.ShapeDtypeStruct((r, c), F32), compiler_params=_params("parallel"),
    )(me, stack, recv)


def _sum_block(own, recv):
    r, c = own.shape
    tr = _row_tile(r)

    def body(own_ref, recv_ref, o_ref):
        o_ref[...] = (((own_ref[...] + recv_ref[0].astype(F32)) + recv_ref[1].astype(F32))
                      + recv_ref[2].astype(F32))

    return pl.pallas_call(
        body, name="sum_block", grid=(r // tr,),
        in_specs=[pl.BlockSpec((tr, c), lambda i: (i, 0)), pl.BlockSpec((len(CHIP_FLIPS), tr, c), lambda i: (0, i, 0))],
        out_specs=pl.BlockSpec((tr, c), lambda i: (i, 0)),
        out_shape=jax.ShapeDtypeStruct((r, c), F32), compiler_params=_params("parallel"),
    )(own, recv)


def _adamw_math(w, g, m, v):
    m = ADAM_B1 * m + (1.0 - ADAM_B1) * g
    v = ADAM_B2 * v + (1.0 - ADAM_B2) * (g * g)
    m_hat = m / (1.0 - ADAM_B1 ** ADAM_STEP)
    v_hat = v / (1.0 - ADAM_B2 ** ADAM_STEP)
    delta = -ADAM_LR * (m_hat / (jnp.sqrt(v_hat) + ADAM_EPS) + ADAM_WD * w)
    return delta, m, v


def _adamw(w, m, v, g_parts, name):
    r, c = w.shape
    tr = _row_tile(r)
    n = len(g_parts)

    def body(*refs):
        w_ref, m_ref, v_ref = refs[:3]
        g_refs = refs[3:3 + n]
        g_out, d_out, m_out, v_out = refs[3 + n:]
        g = g_refs[0][...]
        for ref in g_refs[1:]:
            g = g + ref[...]
        g_out[...] = g
        d_out[...], m_out[...], v_out[...] = _adamw_math(w_ref[...], g, m_ref[...], v_ref[...])

    blk = pl.BlockSpec((tr, c), lambda i: (i, 0))
    return pl.pallas_call(
        body, name=name, grid=(r // tr,), in_specs=[blk] * (3 + n), out_specs=[blk] * 4,
        out_shape=[jax.ShapeDtypeStruct((r, c), F32)] * 4, compiler_params=_params("parallel"),
    )(w, m, v, *g_parts)


def _adamw_small(total, w, m, v):
    sizes = [w[n].size for n in SMALL]
    flat = lambda d: [d[n].reshape(1, -1) for n in SMALL]
    k = len(SMALL)

    def body(*refs):
        total_ref, w_refs, m_refs, v_refs = refs[0], refs[1:1 + k], refs[1 + k:1 + 2 * k], refs[1 + 2 * k:1 + 3 * k]
        outs = refs[1 + 3 * k:]
        for i, size in enumerate(sizes):
            g = total_ref[i:i + 1, 0:size]
            outs[i][...] = g
            outs[k + i][...], outs[2 * k + i][...], outs[3 * k + i][...] = _adamw_math(
                w_refs[i][...], g, m_refs[i][...], v_refs[i][...])

    res = pl.pallas_call(
        body, name="adamw_small", out_shape=[jax.ShapeDtypeStruct((1, size), F32) for size in sizes] * 4,
        compiler_params=_params(),
    )(total, *flat(w), *flat(m), *flat(v))
    return [{n: res[j * k + i].reshape(w[n].shape) for i, n in enumerate(SMALL)} for j in range(4)]


SHARDED = ("w_in", "w_lora_up", "a_lora_up", "w_out_a", "w_out_b", "w_out")
ROW_SHARDED = ("w_out",)
SMALL = ("norm_g", "shift_mu", "w0", "a0", "k_k", "k_a", "r_k", "lnx_w", "lnx_b", "f_bias", "q_norm_g", "k_norm_g",
         "final_norm_g")
WEIGHTS = ("norm_g", "w_in", "shift_mu", "w_lora_up", "w0", "a_lora_up", "a0", "k_k", "k_a", "r_k", "lnx_w", "lnx_b",
           "f_bias", "q_norm_g", "k_norm_g", "w_out_a", "w_out_b", "w_out", "final_norm_g")
SLAB_ROWS = 16
SLAB_COLS = SEC


def _to_slab(named, extra=None):
    rows = [jnp.pad(named[n].reshape(1, -1), ((0, 0), (0, SLAB_COLS - named[n].size))) for n in SMALL]
    if extra is not None:
        rows.append(jnp.pad(extra.reshape(1, -1), ((0, 0), (0, SLAB_COLS - extra.size))))
    rows.append(jnp.zeros((SLAB_ROWS - len(rows), SLAB_COLS), F32))
    return jnp.concatenate(rows, axis=0)


def _by_chip(g, name):
    if name in ROW_SHARDED:
        return g.reshape(N_CHIPS, g.shape[0] // N_CHIPS, g.shape[1])
    r, c = g.shape
    return g.reshape(r, N_CHIPS, c // N_CHIPS).transpose(1, 0, 2)


def _from_chips(stack, name):
    if name in ROW_SHARDED:
        return stack.reshape(-1, stack.shape[2])
    _, r, c = stack.shape
    return stack.transpose(1, 0, 2).reshape(r, N_CHIPS * c)


def kernel(x, norm_g, w_in, shift_mu, w_lora_up, w0, a_lora_up, a0, k_k, k_a, r_k, lnx_w, lnx_b, f_bias, q_norm_g, k_norm_g, w_out_a, w_out_b, w_out, final_norm_g, loss_target, m_norm_g, m_w_in, m_shift_mu, m_w_lora_up, m_w0, m_a_lora_up, m_a0, m_k_k, m_k_a, m_r_k, m_lnx_w, m_lnx_b, m_f_bias, m_q_norm_g, m_k_norm_g, m_w_out_a, m_w_out_b, m_w_out, m_final_norm_g, v_norm_g, v_w_in, v_shift_mu, v_w_lora_up, v_w0, v_a_lora_up, v_a0, v_k_k, v_k_a, v_r_k, v_lnx_w, v_lnx_b, v_f_bias, v_q_norm_g, v_k_norm_g, v_w_out_a, v_w_out_b, v_w_out, v_final_norm_g):
    w = dict(norm_g=norm_g, w_in=w_in, shift_mu=shift_mu, w_lora_up=w_lora_up, w0=w0, a_lora_up=a_lora_up, a0=a0,
             k_k=k_k, k_a=k_a, r_k=r_k, lnx_w=lnx_w, lnx_b=lnx_b, f_bias=f_bias, q_norm_g=q_norm_g,
             k_norm_g=k_norm_g, w_out_a=w_out_a, w_out_b=w_out_b, w_out=w_out, final_norm_g=final_norm_g)
    m = dict(norm_g=m_norm_g, w_in=m_w_in, shift_mu=m_shift_mu, w_lora_up=m_w_lora_up, w0=m_w0,
             a_lora_up=m_a_lora_up, a0=m_a0, k_k=m_k_k, k_a=m_k_a, r_k=m_r_k, lnx_w=m_lnx_w, lnx_b=m_lnx_b,
             f_bias=m_f_bias, q_norm_g=m_q_norm_g, k_norm_g=m_k_norm_g, w_out_a=m_w_out_a, w_out_b=m_w_out_b,
             w_out=m_w_out, final_norm_g=m_final_norm_g)
    v = dict(norm_g=v_norm_g, w_in=v_w_in, shift_mu=v_shift_mu, w_lora_up=v_w_lora_up, w0=v_w0,
             a_lora_up=v_a_lora_up, a0=v_a0, k_k=v_k_k, k_a=v_k_a, r_k=v_r_k, lnx_w=v_lnx_w, lnx_b=v_lnx_b,
             f_bias=v_f_bias, q_norm_g=v_q_norm_g, k_norm_g=v_k_norm_g, w_out_a=v_w_out_a, w_out_b=v_w_out_b,
             w_out=v_w_out, final_norm_g=v_final_norm_g)
    shapes = {n: w[n].shape for n in WEIGHTS}

    shard = {n: w[n][0].astype(BF16) for n in SHARDED}
    late = ("w_out_a", "w_out_b", "w_out")
    loras = ("w_lora_up", "a_lora_up")
    w_in_head, w_in_tail = shard["w_in"][:, :A_TAIL], shard["w_in"][:, A_TAIL:]
    shard0, shard1_head, up_stack, aup_stack = _run_on_sequencer(_gather_exchange(
        [(0, shard["w_in"]), (1, w_in_head)], [shard[n] for n in loras], split=(0, 1)), "gather_early", 2)
    moments = (_row_major_copy(m["w_in"][0], "m_w_in_rows"), _row_major_copy(v["w_in"][0], "v_w_in_rows"))
    shard0, moments = lax.optimization_barrier((shard0, moments))
    w_a = jnp.concatenate([shard0, shard1_head], axis=1)

    def late_weights(arrived):
        shard1_tail, shard2, shard3 = arrived[:3]
        w_b = jnp.concatenate([shard1_tail, shard2[:, :B_TAIL], jnp.zeros((D_MODEL, SEC - FOX_REAL), BF16)], axis=1)
        w_g = jnp.concatenate([shard2[:, B_TAIL:], shard3], axis=1)
        return (w_b, w_g, *[_from_chips(s, n) for n, s in zip(late, arrived[3:])])

    own = {}

    def bwd_exchange(dw_b, dw_g, dwa, dwb, dwo):
        own["tail1"] = dw_b[:, :B_HEAD]
        own["block2"] = jnp.concatenate([dw_b[:, B_HEAD:FOX_REAL], dw_g[:, :G_HEAD]], axis=1)
        own["block3"] = dw_g[:, G_HEAD:]
        own.update({n: _by_chip(g, n) for n, g in zip(late, (dwa, dwb, dwo))})
        return _scatter_exchange([(1, own["tail1"].astype(BF16)), (2, own["block2"].astype(BF16)),
                                  (3, own["block3"].astype(BF16))], [own[n].astype(BF16) for n in late])

    def tail_exchange(dw_a, dw_up, da_up):
        own["block0"], own["head1"] = dw_a[:, :SHARD_COLS], dw_a[:, SHARD_COLS:]
        own.update({n: _by_chip(g, n) for n, g in zip(loras, (dw_up, da_up))})
        return _scatter_exchange([(0, own["block0"].astype(BF16)), (1, own["head1"].astype(BF16))],
                                 [own[n].astype(BF16) for n in loras])

    small = {n: w[n] for n in SMALL}
    loss_vec, grad_x, grads, sent, sent_last = _device_grads(
        x[0], loss_target[0], small, w_a, _from_chips(up_stack, "w_lora_up"), _from_chips(aup_stack, "a_lora_up"),
        late_weights, _gather_exchange([(1, w_in_tail), (2, shard["w_in"]), (3, shard["w_in"])], [shard[n] for n in late]),
        bwd_exchange, tail_exchange)

    total = _allreduce_small(_to_slab(grads, extra=loss_vec))
    loss = (0.5 / D_MODEL) * jnp.sum(total[len(SMALL)])
    out_g, out_d, out_m, out_v = _adamw_small(total, w, m, v)

    xpos, ypos, _ = _position()
    me = (2 * xpos + ypos).astype(jnp.int32).reshape(1)
    core_sum = {n: _sum4(own[n], r, me) for n, r in zip(late, sent[3:])}
    theirs = dict(zip(late, _swap_sibling([core_sum[n] for n in late], "swap_sibling_early")))
    sent_last, out_d["norm_g"], theirs = lax.optimization_barrier((sent_last, out_d["norm_g"], theirs))
    core_sum["w_in"] = lax.switch(me[0], [
        lambda: _sum_block(own["block0"], sent_last[0]),
        lambda: jnp.concatenate([_sum_block(own["head1"], sent_last[1]), _sum_block(own["tail1"], sent[0])], axis=1),
        lambda: _sum_block(own["block2"], sent[1]),
        lambda: _sum_block(own["block3"], sent[2])])
    core_sum.update({n: _sum4(own[n], r, me) for n, r in zip(loras, sent_last[2:])})
    rest = ("w_in",) + loras
    theirs.update(zip(rest, _swap_sibling([core_sum[n] for n in rest], "swap_sibling")))
    for n in SHARDED:
        m_n, v_n = moments if n == "w_in" else (m[n][0], v[n][0])
        g, d, m2, v2 = _adamw(w[n][0], m_n, v_n, [core_sum[n], theirs[n]], "adamw_" + n)
        out_g[n], out_d[n], out_m[n], out_v[n] = (a.reshape(shapes[n]) for a in (g, d, m2, v2))

    return (loss, grad_x.reshape(x.shape), *[out_g[n] for n in WEIGHTS], *[out_d[n] for n in WEIGHTS],
            *[out_m[n] for n in WEIGHTS], *[out_v[n] for n in WEIGHTS])
```

```python
import functools
import math

import jax
import jax.numpy as jnp
from jax import lax
from jax.experimental import pallas as pl
from jax.experimental.pallas import tpu as pltpu
from jax.experimental.pallas import tpu_sc as plsc

F32 = jnp.float32
BF16 = jnp.bfloat16

D_MODEL = 1024
D_HALF = 512
HEAD = 64
N_HEADS = 8
LORA = 64
RWKV_COLS = 2176
FOX_REAL = 2056
SEC = 2176
GATE_COLS = 2048
IN_COLS = 6280
N_CHIPS = 4
SHARD_COLS = IN_COLS // N_CHIPS
A_TAIL = RWKV_COLS - SHARD_COLS
B_HEAD = SHARD_COLS - A_TAIL
B_TAIL = FOX_REAL - B_HEAD
G_HEAD = SHARD_COLS - B_TAIL
RMS_EPS = 1e-6
LNX_EPS = 64e-5
ATT_SCALE = HEAD ** -0.5
NEG = -1e30

ADAM_LR = 0.001
ADAM_B1 = 0.9
ADAM_B2 = 0.999
ADAM_EPS = 1e-08
ADAM_WD = 0.01
ADAM_STEP = 10

LANES = 128
SUBLANES = 8
VMEM_LIMIT = 56 * 1024 * 1024
MESH = pl.DeviceIdType.MESH


def _params(*sem):
    return pltpu.CompilerParams(dimension_semantics=sem if sem else None, vmem_limit_bytes=VMEM_LIMIT)


def _sigmoid(x):
    return 1.0 / (1.0 + jnp.exp(-x))


def _log_sigmoid(x):
    return jnp.minimum(x, 0.0) - jnp.log(1.0 + jnp.exp(-jnp.abs(x)))


def _head_ones():
    r = lax.broadcasted_iota(jnp.int32, (LANES, LANES), 0) >> 6
    c = lax.broadcasted_iota(jnp.int32, (LANES, LANES), 1) >> 6
    return (r == c).astype(BF16)


def _split3(x):
    hi = x.astype(BF16)
    r1 = x - hi.astype(F32)
    mid = r1.astype(BF16)
    lo = (r1 - mid.astype(F32)).astype(BF16)
    return hi, mid, lo


def _exact_dot(x, ones_bf16, ones_first=False):
    out = None
    for piece in _split3(x):
        if ones_first:
            t = jnp.dot(ones_bf16, piece, preferred_element_type=F32)
        else:
            t = jnp.dot(piece, ones_bf16, preferred_element_type=F32)
        out = t if out is None else out + t
    return out


def _head_sum(x, bd):
    n = x.shape[1] // LANES
    parts = [_exact_dot(x[:, i * LANES:(i + 1) * LANES], bd) for i in range(n)]
    return parts[0] if n == 1 else jnp.concatenate(parts, axis=1)


def _dot_nt(a, b):
    return lax.dot_general(a, b, (((1,), (1,)), ((), ())), preferred_element_type=F32)


def _dot_tn(a, b):
    return lax.dot_general(a, b, (((0,), (0,)), ((), ())), preferred_element_type=F32)


def _colsum(x):
    return jnp.sum(x, axis=0, keepdims=True)


def _matmul_tn_acc(at, b, name, tk=512):
    m, k = at.shape
    n = b.shape[1]

    def body(a_ref, b_ref, o_ref):
        j = pl.program_id(0)

        @pl.when(j == 0)
        def _():
            o_ref[...] = jnp.zeros_like(o_ref)

        o_ref[...] += jnp.dot(a_ref[...], b_ref[...].astype(BF16), preferred_element_type=F32)

    return pl.pallas_call(
        body, name=name, grid=(k // tk,),
        in_specs=[pl.BlockSpec((m, tk), lambda j: (0, j)), pl.BlockSpec((tk, n), lambda j: (j, 0))],
        out_specs=pl.BlockSpec((m, n), lambda j: (0, 0)),
        out_shape=jax.ShapeDtypeStruct((m, n), F32), compiler_params=_params("arbitrary"),
    )(at, b)


def _inproj_bwd(du_a, du_b, du_g, w_a, w_b, w_g, x, dx2, g, exchange=None, tm=256):
    s, d = x.shape
    nb = s // tm

    def body(*refs):
        ((da_ref, db_ref, dg_ref, wa_ref, wb_ref, wg_ref, x_ref, dx2_ref, g_ref), (gx_ref, gg_ref), _,
         moves) = _split_refs(refs, 9, 2, exchange)
        i = pl.program_id(0)
        if moves:
            moves.start(also=(i == 0))

        @pl.when(i == 0)
        def _():
            gg_ref[...] = jnp.zeros_like(gg_ref)

        dh = _dot_nt(da_ref[...].astype(BF16), wa_ref[...])
        dh += _dot_nt(db_ref[...].astype(BF16), wb_ref[...])
        dh += _dot_nt(dg_ref[...].astype(BF16), wg_ref[...])
        xv = x_ref[...]
        r = lax.rsqrt(jnp.mean(xv * xv, axis=-1, keepdims=True) + RMS_EPS)
        xh = xv * r
        gg_ref[...] += _colsum(dh * xh)
        dxh = dh * g_ref[...]
        gx_ref[...] = dx2_ref[...] + r * (dxh - xh * jnp.mean(dxh * xh, axis=-1, keepdims=True))
        if moves:
            moves.wait(also=(i == nb - 1))

    row = lambda w: pl.BlockSpec((tm, w), lambda i: (i, 0))
    full = lambda a: pl.BlockSpec(a.shape, lambda i: (0, 0))
    ex_in = exchange.operands if exchange else []
    ex_out = exchange.out_shapes if exchange else []
    res = pl.pallas_call(
        body, name="inproj_bwd", grid=(nb,),
        in_specs=[row(SEC), row(SEC), row(GATE_COLS), full(w_a), full(w_b), full(w_g), row(d), row(d), full(g)]
                 + [ANY] * len(ex_in),
        out_specs=[row(d), pl.BlockSpec((1, d), lambda i: (0, 0))] + [ANY] * len(ex_out),
        out_shape=[jax.ShapeDtypeStruct((s, d), F32), jax.ShapeDtypeStruct((1, d), F32)] + ex_out,
        scratch_shapes=exchange.scratch() if exchange else [],
        compiler_params=_params("arbitrary"),
    )(du_a, du_b, du_g, w_a, w_b, w_g, x, dx2, g, *ex_in)
    return res[0], res[1], list(res[2:])


def _rwkv_elementwise(ua, prev_row, first, mu, wl, w0, a0, kkw, kaw, bd):
    tm = ua.shape[0]
    rows = lax.broadcasted_iota(jnp.int32, (tm, 1), 0)
    prev = jnp.where(first, jnp.zeros_like(prev_row), prev_row)
    shifted = jnp.where(rows == 0, prev, pltpu.roll(ua, 1, 0))
    delta = shifted - ua
    us = ua + delta * mu
    r = us[:, 0:512]
    k0 = us[:, 512:1024]
    v = us[:, 1024:1536]
    lo = us[:, 1536:1664]
    gate = us[:, 1664:2176]
    lane = lax.broadcasted_iota(jnp.int32, (1, LANES), 1)
    th = jnp.tanh(lo)
    lin = jnp.where(lane < LORA, th, lo)
    ll = jnp.dot(lin.astype(BF16), wl, preferred_element_type=F32)
    sz = _sigmoid(w0 + ll[:, :512])
    e = sz * math.exp(-0.5)
    dec = jnp.exp(-e)
    a = _sigmoid(a0 + ll[:, 512:])
    kk0 = k0 * kkw
    ss = _head_sum(kk0 * kk0, bd)
    nrm = jnp.maximum(jnp.sqrt(ss), 1e-12)
    kk = kk0 / nrm
    k = k0 * (1.0 + (a - 1.0) * kaw)
    return dict(delta=delta, us=us, r=r, k0=k0, v=v, lo=lo, gate=gate, th=th, lin=lin, sz=sz, e=e, dec=dec,
                a=a, kk0=kk0, ss=ss, nrm=nrm, kk=kk, k=k)


def _rwkv_front(x, g, w_a, mu, wl, w0, a0, kkw, kaw, tm=256):
    s, d = x.shape

    def body(x_ref, g_ref, wa_ref, mu_ref, wl_ref, w0_ref, a0_ref, kkw_ref, kaw_ref,
             h_ref, ua_ref, r_ref, w_ref, k_ref, v_ref, a_ref, b_ref, gate_ref, last_row):
        i = pl.program_id(0)
        xv = x_ref[...]
        h = (xv * lax.rsqrt(jnp.mean(xv * xv, axis=-1, keepdims=True) + RMS_EPS) * g_ref[...]).astype(BF16)
        h_ref[...] = h
        ua = jnp.dot(h, wa_ref[...], preferred_element_type=F32)
        ua_ref[...] = ua

        @pl.when(i == 0)
        def _():
            last_row[...] = jnp.zeros_like(last_row)

        f = _rwkv_elementwise(ua, last_row[...], i == 0, mu_ref[...], wl_ref[...], w0_ref[...],
                              a0_ref[...], kkw_ref[...], kaw_ref[...], _head_ones())
        last_row[...] = ua[tm - 1:tm, :]
        r_ref[...] = f["r"]
        w_ref[...] = f["dec"]
        k_ref[...] = f["k"]
        v_ref[...] = f["v"]
        a_ref[...] = -f["kk"]
        b_ref[...] = f["kk"] * f["a"]
        gate_ref[...] = f["gate"]

    vec = lambda w: pl.BlockSpec((1, w), lambda i: (0, 0))
    row = lambda w: pl.BlockSpec((tm, w), lambda i: (i, 0))
    return pl.pallas_call(
        body, name="rwkv_front", grid=(s // tm,),
        in_specs=[row(d), vec(d), pl.BlockSpec(w_a.shape, lambda i: (0, 0), pipeline_mode=pl.Buffered(1)),
                  vec(SEC), pl.BlockSpec((LANES, 2 * D_HALF), lambda i: (0, 0)),
                  vec(D_HALF), vec(D_HALF), vec(D_HALF), vec(D_HALF)],
        out_specs=[row(d), row(SEC)] + [row(D_HALF)] * 7,
        out_shape=[jax.ShapeDtypeStruct((s, d), BF16), jax.ShapeDtypeStruct((s, SEC), F32)]
                  + [jax.ShapeDtypeStruct((s, D_HALF), F32)] * 7,
        scratch_shapes=[pltpu.VMEM((1, SEC), F32)],
        compiler_params=_params("arbitrary"),
    )(x, g, w_a, mu, wl, w0, a0, kkw, kaw)


SCAN_TB = 128
N_PAIRS = 4


def _pair_sum(x, left):
    s_l = jnp.sum(jnp.where(left, x, 0.0), axis=1, keepdims=True)
    s_r = jnp.sum(jnp.where(left, 0.0, x), axis=1, keepdims=True)
    return jnp.where(left, s_l, s_r)


def _pair_dot(x, row_l, row_r, left):
    s_l = jnp.sum(x * row_l, axis=1, keepdims=True)
    s_r = jnp.sum(x * row_r, axis=1, keepdims=True)
    return jnp.where(left, s_l, s_r)


def _halves(rows8):
    lane = lax.broadcasted_iota(jnp.int32, rows8.shape, 1)
    keep_left = (lane & (LANES - 1)) < HEAD
    return jnp.where(keep_left, rows8, 0.0), jnp.where(keep_left, 0.0, rows8)


def _quad_consts():
    lane = lax.broadcasted_iota(jnp.int32, (HEAD, 2 * LANES), 1)
    rowi = lax.broadcasted_iota(jnp.int32, (HEAD, 2 * LANES), 0)
    diag2 = rowi == (lane & (HEAD - 1))
    r = lax.broadcasted_iota(jnp.int32, (2 * LANES, 2 * LANES), 0) >> 6
    c = lax.broadcasted_iota(jnp.int32, (2 * LANES, 2 * LANES), 1) >> 6
    return diag2, (r == c).astype(BF16)


def _rows_to_columns(x8, diag2, bd2):
    lhs = jnp.concatenate([jnp.where(diag2, x8[i:i + 1], 0.0).astype(BF16) for i in range(SUBLANES)], axis=0)
    return jnp.dot(lhs, bd2, preferred_element_type=F32)


def _diag_rows(qtile, diag2, bd2, sub_row2):
    res = jnp.dot(qtile, bd2, preferred_element_type=F32)
    out = jnp.zeros((SUBLANES, 2 * LANES), F32)
    for i in range(SUBLANES):
        out = jnp.where(sub_row2 == i, _colsum(jnp.where(diag2, res[i * HEAD:(i + 1) * HEAD], 0.0)), out)
    return out


def _store_tile(qbuf, slot, p, i, x):
    qbuf[slot, p // 2, i * HEAD:(i + 1) * HEAD, (p % 2) * LANES:(p % 2 + 1) * LANES] = x.astype(BF16)


def _left_half():
    return lax.broadcasted_iota(jnp.int32, (HEAD, LANES), 1) < HEAD


def _split_refs(refs, n_rows, n_out, exchange):
    n_in = len(exchange.operands) if exchange else 0
    n_ex_out = len(exchange.out_shapes) if exchange else 0
    refs = list(refs)
    rows, refs = refs[:n_rows], refs[n_rows:]
    ex_in, refs = refs[:n_in], refs[n_in:]
    outs, refs = refs[:n_out], refs[n_out:]
    ex_out, refs = refs[:n_ex_out], refs[n_ex_out:]
    scratch, sems = (refs[:-3], refs[-3:]) if exchange else (refs, None)
    moves = exchange.moves(ex_in, ex_out, sems) if exchange else None
    return rows, outs, scratch, moves


def _wkv_fwd(r, w, k, a, b, v, exchange=None):
    s = r.shape[0]
    tb = SCAN_TB
    nb = s // tb

    def body(*refs):
        (r_ref, w_ref, k_ref, a_ref, b_ref, v_ref), (y_ref, st_ref), (state, vbuf, qbuf), moves = _split_refs(
            refs, 6, 2, exchange)
        g = pl.program_id(0)
        if moves:
            moves.start(also=(g == 0))

        @pl.when(g == 0)
        def _():
            state[...] = jnp.zeros_like(state)
            qbuf[...] = jnp.zeros_like(qbuf)

        left = _left_half()
        diag2, bd2 = _quad_consts()
        sub_row2 = lax.broadcasted_iota(jnp.int32, (SUBLANES, 2 * LANES), 0)
        groups = tb // SUBLANES
        quads = [slice(g2 * 2 * LANES, (g2 + 1) * 2 * LANES) for g2 in range(2)]

        def rows_of(q):
            return pl.ds(pl.multiple_of(q * SUBLANES, SUBLANES), SUBLANES)

        def v_tiles(q, slot):
            v8 = v_ref[rows_of(q), :]
            for g2 in range(2):
                vbuf[slot, g2] = _rows_to_columns(v8[:, quads[g2]], diag2, bd2)

        def chain(q, slot):
            rows8 = rows_of(q)
            a8, w8, b8, k8, r8 = (x[rows8, :] for x in (a_ref, w_ref, b_ref, k_ref, r_ref))
            pairs = [slice(p * LANES, (p + 1) * LANES) for p in range(N_PAIRS)]
            a_next = pltpu.roll(a8, SUBLANES - 1, 0)
            (a8_l, a8_r), (wa8_l, wa8_r) = _halves(a8), _halves(w8 * a_next)
            ba8 =jnp.concatenate([_pair_sum(b8[:, pr] * a_next[:, pr], left[0:SUBLANES]) for pr in pairs], axis=1)
            ka8 = jnp.concatenate([_pair_sum(k8[:, pr] * a_next[:, pr], left[0:SUBLANES]) for pr in pairs], axis=1)
            sp = [state[p] for p in range(N_PAIRS)]
            for i in range(0, SUBLANES, 2):
                r0, r1 = slice(i, i + 1), slice(i + 1, i + 2)
                sums = [(_pair_dot(sp[p], a8_l[r0, pairs[p]], a8_r[r0, pairs[p]], left),
                         _pair_dot(sp[p], wa8_l[r0, pairs[p]], wa8_r[r0, pairs[p]], left)) for p in range(N_PAIRS)]
                sa0, sa1 = [s[0] for s in sums], [s[1] for s in sums]
                for p in range(N_PAIRS):
                    pr = pairs[p]
                    inner = slice((p % 2) * LANES, (p % 2 + 1) * LANES)
                    vt0 = vbuf[slot, p // 2, i * HEAD:(i + 1) * HEAD, inner]
                    vt1 = vbuf[slot, p // 2, (i + 1) * HEAD:(i + 2) * HEAD, inner]
                    sa_next = sa1[p] + sa0[p] * ba8[r0, pr] + vt0 * ka8[r0, pr]
                    s1 = sp[p] * w8[r0, pr] + sa0[p] * b8[r0, pr] + vt0 * k8[r0, pr]
                    st_ref[q * SUBLANES + i, p] = s1
                    _store_tile(qbuf, slot, p, i, s1 * r8[r0, pr])
                    s2 = s1 * w8[r1, pr] + sa_next * b8[r1, pr] + vt1 * k8[r1, pr]
                    st_ref[q * SUBLANES + i + 1, p] = s2
                    _store_tile(qbuf, slot, p, i + 1, s2 * r8[r1, pr])
                    sp[p] = s2
            for p in range(N_PAIRS):
                state[p] = sp[p]

        def y_rows(q, slot):
            for g2 in range(2):
                y_ref[rows_of(q), quads[g2]] = _diag_rows(qbuf[slot, g2], diag2, bd2, sub_row2)

        v_tiles(0, 0)

        def two_groups(j, carry):
            q0 = 2 * j
            v_tiles(q0 + 1, 1)
            chain(q0, 0)
            y_rows(jnp.maximum(q0 - 1, 0), 1)
            v_tiles(jnp.minimum(q0 + 2, groups - 1), 0)
            chain(q0 + 1, 1)
            y_rows(q0, 0)
            return carry

        lax.fori_loop(0, groups // 2, two_groups, 0)
        y_rows(groups - 1, 1)
        if moves:
            moves.wait(also=(g == nb - 1))

    rows = pl.BlockSpec((tb, D_HALF), lambda g: (g, 0))
    ex_in = exchange.operands if exchange else []
    ex_out = exchange.out_shapes if exchange else []
    res = pl.pallas_call(
        body, name="wkv_fwd", grid=(nb,),
        in_specs=[rows] * 6 + [ANY] * len(ex_in),
        out_specs=[rows, pl.BlockSpec((tb, N_PAIRS, HEAD, LANES), lambda g: (g, 0, 0, 0))] + [ANY] * len(ex_out),
        out_shape=[jax.ShapeDtypeStruct((s, D_HALF), F32),
                   jax.ShapeDtypeStruct((s, N_PAIRS, HEAD, LANES), F32)] + ex_out,
        scratch_shapes=[pltpu.VMEM((N_PAIRS, HEAD, LANES), F32),
                        pltpu.VMEM((2, 2, SUBLANES * HEAD, 2 * LANES), F32),
                        pltpu.VMEM((2, 2, SUBLANES * HEAD, 2 * LANES), BF16)]
                       + (exchange.scratch() if exchange else []),
        compiler_params=_params("arbitrary"),
    )(r, w, k, a, b, v, *ex_in)
    return res[0], res[1], list(res[2:])


def _wkv_bwd(r, w, k, a, b, v, dy, st, exchange=None):
    s = r.shape[0]
    tb = SCAN_TB
    nb = s // tb

    def body(*refs):
        ((r_ref, w_ref, k_ref, a_ref, b_ref, v_ref, dy_ref, st_ref, before_ref),
         (dr_ref, dw_ref, dk_ref, dv_ref, da_ref, db_ref), (dstate, vbuf, qbuf, sbuf),
         moves) = _split_refs(refs, 9, 6, exchange)
        g = pl.program_id(0)
        first_block = g == nb - 1
        if moves:
            moves.start(also=(g == 0))

        @pl.when(g == 0)
        def _():
            dstate[...] = jnp.zeros_like(dstate)
            qbuf[...] = jnp.zeros_like(qbuf)

        left = _left_half()
        diag2, bd2 = _quad_consts()
        sub_row = lax.broadcasted_iota(jnp.int32, (SUBLANES, LANES), 0)
        sub_row2 = lax.broadcasted_iota(jnp.int32, (SUBLANES, 2 * LANES), 0)
        groups = tb // SUBLANES
        quads = [slice(g2 * 2 * LANES, (g2 + 1) * 2 * LANES) for g2 in range(2)]
        row_refs = (dr_ref, dw_ref, dk_ref, da_ref, db_ref)

        def rows_of(q):
            return pl.ds(pl.multiple_of(q * SUBLANES, SUBLANES), SUBLANES)

        def state_before(q, i, p):
            if i > 0:
                return st_ref[q * SUBLANES + i - 1, p]
            return jnp.where(q == 0, jnp.where(first_block, 0.0, before_ref[0, p]),
                             st_ref[jnp.maximum(q * SUBLANES - 1, 0), p])

        def column_tiles(q, slot):
            rows8 = rows_of(q)
            for kind, ref in enumerate((v_ref, dy_ref)):
                x8 = ref[rows8, :]
                for g2 in range(2):
                    vbuf[slot, kind, g2] = _rows_to_columns(x8[:, quads[g2]], diag2, bd2)
            a8 = a_ref[rows8, :]
            for i in range(SUBLANES):
                for p in range(N_PAIRS):
                    _store_tile(sbuf, 0, p, i, state_before(q, i, p) * a8[i:i + 1, p * LANES:(p + 1) * LANES])
            for g2 in range(2):
                vbuf[slot, 2, g2] = jnp.dot(sbuf[0, g2], bd2, preferred_element_type=F32)

        def chain(q, slot):
            rows8 = rows_of(q)
            a8, w8, b8, k8, r8 = (x[rows8, :] for x in (a_ref, w_ref, b_ref, k_ref, r_ref))
            b8_l, b8_r = _halves(b8)
            dsp = [dstate[p] for p in range(N_PAIRS)]
            outs = [[jnp.zeros((SUBLANES, LANES), F32) for _ in row_refs] for _ in range(N_PAIRS)]
            after = [st_ref[q * SUBLANES + SUBLANES - 1, p] for p in range(N_PAIRS)]
            for i in reversed(range(SUBLANES)):
                row = slice(i, i + 1)
                pl_ = [slice(p * LANES, (p + 1) * LANES) for p in range(N_PAIRS)]
                tile = [(p // 2, slice(i * HEAD, (i + 1) * HEAD), slice((p % 2) * LANES, (p % 2 + 1) * LANES))
                        for p in range(N_PAIRS)]
                sp = [state_before(q, i, p) for p in range(N_PAIRS)]
                dyt = [vbuf[(slot, 1) + tile[p]] for p in range(N_PAIRS)]
                ds = [dsp[p] + dyt[p] * r8[row, pl_[p]] for p in range(N_PAIRS)]
                dsa = [_pair_dot(ds[p], b8_l[row, pl_[p]], b8_r[row, pl_[p]], left) for p in range(N_PAIRS)]
                sa = [vbuf[(slot, 2) + tile[p]] for p in range(N_PAIRS)]
                for p in range(N_PAIRS):
                    ar, wr, br, kr = (x[row, pl_[p]] for x in (a8, w8, b8, k8))
                    vt = vbuf[(slot, 0) + tile[p]]
                    dsp[p] = ds[p] * wr + dsa[p] * ar
                    new = (_colsum(after[p] * dyt[p]), _colsum(ds[p] * sp[p]), _colsum(ds[p] * vt),
                           _colsum(sp[p] * dsa[p]), _colsum(ds[p] * sa[p]))
                    outs[p] = [jnp.where(sub_row == i, n, o) for n, o in zip(new, outs[p])]
                    _store_tile(qbuf, slot, p, i, ds[p] * kr)
                after = sp
            for p in range(N_PAIRS):
                dstate[p] = dsp[p]
                for ref, o in zip(row_refs, outs[p]):
                    ref[rows8, p * LANES:(p + 1) * LANES] = o

        def dv_rows(q, slot):
            for g2 in range(2):
                dv_ref[rows_of(q), quads[g2]] = _diag_rows(qbuf[slot, g2], diag2, bd2, sub_row2)

        column_tiles(groups - 1, 0)

        def two_groups(j, carry):
            q0 = groups - 1 - 2 * j
            column_tiles(q0 - 1, 1)
            chain(q0, 0)
            dv_rows(jnp.minimum(q0 + 1, groups - 1), 1)
            column_tiles(jnp.maximum(q0 - 2, 0), 0)
            chain(q0 - 1, 1)
            dv_rows(q0, 0)
            return carry

        lax.fori_loop(0, groups // 2, two_groups, 0)
        dv_rows(0, 1)
        if moves:
            moves.wait(also=(g == nb - 1))

    rows = pl.BlockSpec((tb, D_HALF), lambda g: (nb - 1 - g, 0))
    ex_in = exchange.operands if exchange else []
    ex_out = exchange.out_shapes if exchange else []
    res = pl.pallas_call(
        body, name="wkv_bwd", grid=(nb,),
        in_specs=[rows] * 7 + [pl.BlockSpec((tb, N_PAIRS, HEAD, LANES), lambda g: (nb - 1 - g, 0, 0, 0)),
                               pl.BlockSpec((1, N_PAIRS, HEAD, LANES),
                                            lambda g: (jnp.maximum((nb - 1 - g) * tb - 1, 0), 0, 0, 0))]
                 + [ANY] * len(ex_in),
        out_specs=[rows] * 6 + [ANY] * len(ex_out),
        out_shape=[jax.ShapeDtypeStruct((s, D_HALF), F32)] * 6 + ex_out,
        scratch_shapes=[pltpu.VMEM((N_PAIRS, HEAD, LANES), F32),
                        pltpu.VMEM((2, 3, 2, SUBLANES * HEAD, 2 * LANES), F32),
                        pltpu.VMEM((2, 2, SUBLANES * HEAD, 2 * LANES), BF16),
                        pltpu.VMEM((1, 2, SUBLANES * HEAD, 2 * LANES), BF16)]
                       + (exchange.scratch() if exchange else []),
        compiler_params=_params("arbitrary"),
    )(r, w, k, a, b, v, dy, st, st, *ex_in)
    return list(res[:6]), list(res[6:])


def _rwkv_post_math(y, r, k, v, gate, lw, lb, rk, bd):
    mean = _head_sum(y, bd) * (1.0 / HEAD)
    yc = y - mean
    var = _head_sum(yc * yc, bd) * (1.0 / HEAD)
    rstd = lax.rsqrt(var + LNX_EPS)
    yn = yc * rstd
    rkk = _head_sum(r * k * rk, bd)
    sg = _sigmoid(gate)
    pre = yn * lw + lb + rkk * v
    return yn, rstd, rkk, sg, pre


def _rwkv_prep_bwd(u_a, h_t, grads, mu, wl, w0, a0, kkw, kaw, tm=256):
    s = u_a.shape[0]
    nb = s // tm
    d = h_t.shape[0]

    def body(ua_ref, prev_ref, ht_ref, drs_ref, dws_ref, dks_ref, dvs_ref, das_ref, dbs_ref, drb_ref, dkb_ref, dvb_ref,
             dgt_ref, mu_ref, wl_ref, w0_ref, a0_ref, kkw_ref, kaw_ref,
             du_ref, dwa_ref, dmu_ref, dwl_ref, dw0_ref, da0_ref, dkkw_ref, dkaw_ref, carry):
        i = pl.program_id(0)

        @pl.when(i == 0)
        def _():
            carry[...] = jnp.zeros_like(carry)
            for ref in (dwa_ref, dmu_ref, dwl_ref, dw0_ref, da0_ref, dkkw_ref, dkaw_ref):
                ref[...] = jnp.zeros_like(ref)

        bd = _head_ones()
        mu_v, wl_v, kkw_v, kaw_v = mu_ref[...], wl_ref[...], kkw_ref[...], kaw_ref[...]
        f = _rwkv_elementwise(ua_ref[...], prev_ref[7:8, :], i == nb - 1, mu_v, wl_v, w0_ref[...],
                              a0_ref[...], kkw_v, kaw_v, bd)
        a, kk, k0 = f["a"], f["kk"], f["k0"]
        dk = dks_ref[...] + dkb_ref[...]
        dbs = dbs_ref[...]
        dkk = dbs * a - das_ref[...]
        da = dbs * kk + dk * k0 * kaw_v
        dk0 = dk * (1.0 + (a - 1.0) * kaw_v)
        dkaw_ref[...] += _colsum(dk * k0 * (a - 1.0))
        inv = 1.0 / f["nrm"]
        proj = _head_sum(dkk * kk, bd)
        dkk0 = jnp.where(f["ss"] > 1e-24, (dkk - kk * proj) * inv, dkk * inv)
        dk0 = dk0 + dkk0 * kkw_v
        dkkw_ref[...] += _colsum(dkk0 * k0)
        dza = da * a * (1.0 - a)
        da0_ref[...] += _colsum(dza)
        dz = -dws_ref[...] * f["dec"] * f["e"] * (1.0 - f["sz"])
        dw0_ref[...] += _colsum(dz)
        dll = jnp.concatenate([dz, dza], axis=1).astype(BF16)
        dwl_ref[...] += _dot_tn(f["lin"].astype(BF16), dll)
        dlin = _dot_nt(dll, wl_v)
        lane = lax.broadcasted_iota(jnp.int32, (1, LANES), 1)
        th = f["th"]
        dlo = jnp.where(lane < LORA, dlin * (1.0 - th * th), dlin)
        dus = jnp.concatenate([drs_ref[...] + drb_ref[...], dk0, dvs_ref[...] + dvb_ref[...], dlo, dgt_ref[...]],
                              axis=1)
        dmu_ref[...] += _colsum(dus * f["delta"])
        g1 = dus * mu_v
        rows = lax.broadcasted_iota(jnp.int32, (tm, 1), 0)
        up = jnp.where(rows == tm - 1, carry[...], pltpu.roll(g1, tm - 1, 0))
        dua = dus - g1 + up
        du_ref[...] = dua
        dwa_ref[...] += jnp.dot(ht_ref[...], dua.astype(BF16), preferred_element_type=F32)
        carry[...] = g1[0:1, :]

    rev = lambda w: pl.BlockSpec((tm, w), lambda i: (nb - 1 - i, 0))
    vec = lambda w: pl.BlockSpec((1, w), lambda i: (0, 0))
    wl_spec = pl.BlockSpec((LANES, 2 * D_HALF), lambda i: (0, 0))
    return pl.pallas_call(
        body, name="rwkv_prep_bwd", grid=(nb,),
        in_specs=[rev(SEC), pl.BlockSpec((8, SEC), lambda i: (jnp.maximum((nb - 1 - i) * (tm // 8) - 1, 0), 0)),
                  pl.BlockSpec((d, tm), lambda i: (0, nb - 1 - i))]
                 + [rev(D_HALF)] * 10 + [vec(SEC), wl_spec] + [vec(D_HALF)] * 4,
        out_specs=[rev(SEC), pl.BlockSpec((d, SEC), lambda i: (0, 0)), vec(SEC), wl_spec] + [vec(D_HALF)] * 4,
        out_shape=[jax.ShapeDtypeStruct((s, SEC), F32), jax.ShapeDtypeStruct((d, SEC), F32),
                   jax.ShapeDtypeStruct((1, SEC), F32),
                   jax.ShapeDtypeStruct((LANES, 2 * D_HALF), F32)] + [jax.ShapeDtypeStruct((1, D_HALF), F32)] * 4,
        scratch_shapes=[pltpu.VMEM((1, SEC), F32)],
        compiler_params=_params("arbitrary"),
    )(u_a, u_a, h_t, *grads, mu, wl, w0, a0, kkw, kaw)


def _tri(tm, lower):
    r = lax.broadcasted_iota(jnp.int32, (tm, tm), 0)
    c = lax.broadcasted_iota(jnp.int32, (tm, tm), 1)
    return ((r >= c) if lower else (r <= c)).astype(BF16)


def _head_rms(x, g, bd):
    rinv = lax.rsqrt(_head_sum(x * x, bd) * (1.0 / HEAD) + RMS_EPS)
    xh = x * rinv
    return xh, rinv, xh * g


def _fox_front(h, w_b, fb, qg, kg, tm=256):
    s, d = h.shape

    def body(h_ref, wb_ref, fb_ref, qg_ref, kg_ref, ub_ref, q_ref, k_ref, v_ref, cc_ref, cr_ref, carry):
        i = pl.program_id(0)

        @pl.when(i == 0)
        def _():
            carry[...] = jnp.zeros_like(carry)

        ub_ref[...] = jnp.dot(h_ref[...], wb_ref[...], preferred_element_type=F32)
        bd = _head_ones()
        _, _, qn = _head_rms(ub_ref[:, 0:512], qg_ref[...], bd)
        _, _, kn = _head_rms(ub_ref[:, 512:1024], kg_ref[...], bd)
        q_ref[...] = (qn * ATT_SCALE).astype(BF16)
        k_ref[...] = kn.astype(BF16)
        v_ref[...] = ub_ref[:, 1024:1536].astype(BF16)
        lane = lax.broadcasted_iota(jnp.int32, (1, LANES), 1)
        logf = jnp.where(lane < N_HEADS, _log_sigmoid(ub_ref[:, 2048:2176] + fb_ref[...]), 0.0)
        cum = _exact_dot(logf, _tri(tm, True), ones_first=True) + carry[...]
        for h in range(N_HEADS):
            cc_ref[h] = jnp.broadcast_to(cum[:, h:h + 1], (tm, LANES))
        cr_ref[...] = jnp.transpose(cum)[0:N_HEADS, :]
        carry[...] = cum[tm - 1:tm, :]

    blk = pl.BlockSpec((tm, D_HALF), lambda i: (i, 0))
    return pl.pallas_call(
        body, name="fox_front", grid=(s // tm,),
        in_specs=[pl.BlockSpec((tm, d), lambda i: (i, 0)),
                  pl.BlockSpec(w_b.shape, lambda i: (0, 0), pipeline_mode=pl.Buffered(1)),
                  pl.BlockSpec((1, LANES), lambda i: (0, 0)),
                  pl.BlockSpec((1, D_HALF), lambda i: (0, 0)), pl.BlockSpec((1, D_HALF), lambda i: (0, 0))],
        out_specs=[pl.BlockSpec((tm, SEC), lambda i: (i, 0)), blk, blk, blk,
                   pl.BlockSpec((N_HEADS, tm, LANES), lambda i: (0, i, 0)), pl.BlockSpec((N_HEADS, tm), lambda i: (0, i))],
        out_shape=[jax.ShapeDtypeStruct((s, SEC), F32)] + [jax.ShapeDtypeStruct((s, D_HALF), BF16)] * 3
                  + [jax.ShapeDtypeStruct((N_HEADS, s, LANES), F32), jax.ShapeDtypeStruct((N_HEADS, s), F32)],
        scratch_shapes=[pltpu.VMEM((1, LANES), F32)],
        compiler_params=_params("arbitrary"),
    )(h, w_b, fb, qg, kg)


ATT_T = 256


def _tiles(nblk, by_query):
    if by_query:
        pairs = [(i, j) for i in range(nblk) for j in range(i + 1)]
    else:
        pairs = [(i, j) for j in range(nblk) for i in range(j, nblk)]
    return (jnp.asarray([p[0] for p in pairs], jnp.int32), jnp.asarray([p[1] for p in pairs], jnp.int32))


def _attn_fwd(q, k, v, cc, cr):
    s = q.shape[0]
    t = ATT_T
    nblk = s // t

    def body(qi_ref, kj_ref, q_ref, k_ref, v_ref, cc_ref, cr_ref, o_ref, lse_ref, m_sc, l_sc, acc_sc):
        i = qi_ref[pl.program_id(0)]
        j = kj_ref[pl.program_id(0)]

        @pl.when(j == 0)
        def _():
            m_sc[...] = jnp.full_like(m_sc, NEG)
            l_sc[...] = jnp.zeros_like(l_sc)
            acc_sc[...] = jnp.zeros_like(acc_sc)

        def tile(on_diagonal):
            causal = _causal_tile(t) if on_diagonal else None
            left = lax.broadcasted_iota(jnp.int32, (1, LANES), 1) < HEAD
            for p in range(N_PAIRS):
                lanes = slice(p * LANES, (p + 1) * LANES)
                q2, k2, v2 = q_ref[:, lanes], k_ref[:, lanes], v_ref[:, lanes]
                acc2 = acc_sc[:, lanes]
                for e in range(2):
                    h = 2 * p + e
                    msk = left if e == 0 else jnp.logical_not(left)
                    sc = _dot_nt(jnp.where(msk, q2, jnp.zeros_like(q2)), k2)
                    sc = sc + (_wide(cc_ref[h]) - cr_ref[h:h + 1, :])
                    if on_diagonal:
                        sc = jnp.where(causal, sc, NEG)
                    m_prev = m_sc[h]
                    m_new = jnp.maximum(m_prev, jnp.max(sc, axis=1, keepdims=True))
                    alpha = jnp.exp(m_prev - m_new)
                    pm = jnp.exp(sc - _wide(m_new))
                    l_sc[h] = alpha * l_sc[h] + jnp.sum(pm, axis=1, keepdims=True)
                    m_sc[h] = m_new
                    pv = jnp.dot(pm.astype(BF16), v2, preferred_element_type=F32)
                    acc2 = jnp.where(msk, alpha * acc2 + pv, acc2)
                acc_sc[:, lanes] = acc2

        pl.when(j < i)(functools.partial(tile, False))
        pl.when(j == i)(functools.partial(tile, True))

        @pl.when(j == i)
        def _():
            left = lax.broadcasted_iota(jnp.int32, (1, LANES), 1) < HEAD
            for p in range(N_PAIRS):
                lanes = slice(p * LANES, (p + 1) * LANES)
                inv = jnp.where(left, 1.0 / l_sc[2 * p], 1.0 / l_sc[2 * p + 1])
                o_ref[:, lanes] = acc_sc[:, lanes] * inv
            for h in range(N_HEADS):
                lse_ref[h] = m_sc[h] + jnp.log(l_sc[h])

    qi, kj = _tiles(nblk, by_query=True)
    qblk = pl.BlockSpec((t, D_HALF), lambda n, qi, kj: (qi[n], 0))
    kblk = pl.BlockSpec((t, D_HALF), lambda n, qi, kj: (kj[n], 0))
    qrep = pl.BlockSpec((N_HEADS, t, LANES), lambda n, qi, kj: (0, qi[n], 0))
    return pl.pallas_call(
        body, name="fox_attn_fwd",
        grid_spec=pltpu.PrefetchScalarGridSpec(
            num_scalar_prefetch=2, grid=(qi.shape[0],),
            in_specs=[qblk, kblk, kblk, qrep, pl.BlockSpec((N_HEADS, t), lambda n, qi, kj: (0, kj[n]))],
            out_specs=[qblk, qrep],
            scratch_shapes=[pltpu.VMEM((N_HEADS, t, LANES), F32), pltpu.VMEM((N_HEADS, t, LANES), F32),
                            pltpu.VMEM((t, D_HALF), F32)]),
        out_shape=[jax.ShapeDtypeStruct((s, D_HALF), F32), jax.ShapeDtypeStruct((N_HEADS, s, LANES), F32)],
        compiler_params=_params("arbitrary"),
    )(qi, kj, q, k, v, cc, cr)


def _causal_tile(t):
    return lax.broadcasted_iota(jnp.int32, (t, t), 0) >= lax.broadcasted_iota(jnp.int32, (t, t), 1)


def _wide(x):
    return jnp.concatenate([x, x], axis=1)


def _attn_probs(q2, k2, v2, do2, msk, causal, bias, lse_rows):
    zero = jnp.zeros_like(q2)
    qh = jnp.where(msk, q2, zero)
    doh = jnp.where(msk, do2, zero)
    sc = _dot_nt(qh, k2) + bias
    if causal is not None:
        sc = jnp.where(causal, sc, NEG)
    pm = jnp.exp(sc - _wide(lse_rows))
    dp = _dot_nt(doh, v2)
    return qh, doh, pm, dp


def _attn_bwd_rowdot(q, k, v, do, lse, cc, cr):
    s = q.shape[0]
    t = ATT_T
    nblk = s // t

    def body(qi_ref, kj_ref, q_ref, k_ref, v_ref, do_ref, lse_ref, cc_ref, cr_ref, dd_ref, acc):
        i = qi_ref[pl.program_id(0)]
        j = kj_ref[pl.program_id(0)]

        @pl.when(j == 0)
        def _():
            acc[...] = jnp.zeros_like(acc)

        def tile(on_diagonal):
            causal = _causal_tile(t) if on_diagonal else None
            left = lax.broadcasted_iota(jnp.int32, (1, LANES), 1) < HEAD
            for p in range(N_PAIRS):
                lanes = slice(p * LANES, (p + 1) * LANES)
                q2, k2, v2, do2 = q_ref[:, lanes], k_ref[:, lanes], v_ref[:, lanes], do_ref[:, lanes]
                for e in range(2):
                    h = 2 * p + e
                    msk = left if e == 0 else jnp.logical_not(left)
                    bias = _wide(cc_ref[h]) - cr_ref[h:h + 1, :]
                    _, _, pm, dp = _attn_probs(q2, k2, v2, do2, msk, causal, bias, lse_ref[h])
                    acc[h] += jnp.sum(pm * dp, axis=1, keepdims=True)

        pl.when(j < i)(functools.partial(tile, False))
        pl.when(j == i)(functools.partial(tile, True))

        @pl.when(j == i)
        def _():
            dd_ref[...] = acc[...]

    qi, kj = _tiles(nblk, by_query=True)
    qblk = pl.BlockSpec((t, D_HALF), lambda n, qi, kj: (qi[n], 0))
    qcol = pl.BlockSpec((N_HEADS, t, LANES), lambda n, qi, kj: (0, qi[n], 0))
    kblk = pl.BlockSpec((t, D_HALF), lambda n, qi, kj: (kj[n], 0))
    return pl.pallas_call(
        body, name="fox_attn_rowdot",
        grid_spec=pltpu.PrefetchScalarGridSpec(
            num_scalar_prefetch=2, grid=(qi.shape[0],),
            in_specs=[qblk, kblk, kblk, qblk, qcol, qcol, pl.BlockSpec((N_HEADS, t), lambda n, qi, kj: (0, kj[n]))],
            out_specs=qcol, scratch_shapes=[pltpu.VMEM((N_HEADS, t, LANES), F32)]),
        out_shape=jax.ShapeDtypeStruct((N_HEADS, s, LANES), F32),
        compiler_params=_params("arbitrary"),
    )(qi, kj, q, k, v, do, lse, cc, cr)


def _attn_bwd(q, k, v, do, lse, dd, cc, cr):
    s = q.shape[0]
    t = ATT_T
    nblk = s // t

    def body(qi_ref, kj_ref, q_ref, k_ref, v_ref, do_ref, lse_ref, dd_ref, cc_ref, cr_ref,
             dq_ref, dk_ref, dv_ref, dcr_ref, dk_sc, dv_sc, dcr_sc):
        i = qi_ref[pl.program_id(0)]
        j = kj_ref[pl.program_id(0)]

        @pl.when(pl.program_id(0) == 0)
        def _():
            dq_ref[...] = jnp.zeros_like(dq_ref)

        @pl.when(i == j)
        def _():
            dk_sc[...] = jnp.zeros_like(dk_sc)
            dv_sc[...] = jnp.zeros_like(dv_sc)
            dcr_sc[...] = jnp.zeros_like(dcr_sc)

        def tile(on_diagonal):
            causal = _causal_tile(t) if on_diagonal else None
            left = lax.broadcasted_iota(jnp.int32, (1, LANES), 1) < HEAD
            qrows = pl.ds(pl.multiple_of(i * t, t), t)
            for p in range(N_PAIRS):
                lanes = slice(p * LANES, (p + 1) * LANES)
                q2, k2, v2, do2 = q_ref[:, lanes], k_ref[:, lanes], v_ref[:, lanes], do_ref[:, lanes]
                zero = jnp.zeros_like(q2)
                dq2 = jnp.zeros((t, LANES), F32)
                dk2 = jnp.zeros((t, LANES), F32)
                dv2 = jnp.zeros((t, LANES), F32)
                for e in range(2):
                    h = 2 * p + e
                    msk = left if e == 0 else jnp.logical_not(left)
                    bias = _wide(cc_ref[h]) - cr_ref[h:h + 1, :]
                    qh, doh, pm, dp = _attn_probs(q2, k2, v2, do2, msk, causal, bias, lse_ref[h])
                    dsc = pm * (dp - _wide(dd_ref[h]))
                    dsb = dsc.astype(BF16)
                    dv2 += _dot_tn(pm.astype(BF16), doh)
                    dk2 += _dot_tn(dsb, qh)
                    dq2 += jnp.dot(dsb, jnp.where(msk, k2, zero), preferred_element_type=F32)
                    dcr_sc[h:h + 1, :] += -_colsum(dsc)
                dq_ref[qrows, lanes] += dq2 * ATT_SCALE
                dk_sc[:, lanes] += dk2
                dv_sc[:, lanes] += dv2

        pl.when(i > j)(functools.partial(tile, False))
        pl.when(i == j)(functools.partial(tile, True))

        @pl.when(i == nblk - 1)
        def _():
            dk_ref[...] = dk_sc[...]
            dv_ref[...] = dv_sc[...]
            dcr_ref[...] = dcr_sc[...]

    qi, kj = _tiles(nblk, by_query=False)
    qblk = pl.BlockSpec((t, D_HALF), lambda n, qi, kj: (qi[n], 0))
    qcol = pl.BlockSpec((N_HEADS, t, LANES), lambda n, qi, kj: (0, qi[n], 0))
    kblk = pl.BlockSpec((t, D_HALF), lambda n, qi, kj: (kj[n], 0))
    krow = pl.BlockSpec((N_HEADS, t), lambda n, qi, kj: (0, kj[n]))
    return pl.pallas_call(
        body, name="fox_attn_bwd",
        grid_spec=pltpu.PrefetchScalarGridSpec(
            num_scalar_prefetch=2, grid=(qi.shape[0],),
            in_specs=[qblk, kblk, kblk, qblk, qcol, qcol, qcol, krow],
            out_specs=[pl.BlockSpec((s, D_HALF), lambda n, qi, kj: (0, 0)), kblk, kblk, krow],
            scratch_shapes=[pltpu.VMEM((t, D_HALF), F32), pltpu.VMEM((t, D_HALF), F32), pltpu.VMEM((N_HEADS, t), F32)]),
        out_shape=[jax.ShapeDtypeStruct((s, D_HALF), F32)] * 3 + [jax.ShapeDtypeStruct((N_HEADS, s), F32)],
        compiler_params=_params("arbitrary"),
    )(qi, kj, q, k, v, do, lse, dd, cc, cr)


def _fox_prep_bwd(u_b, h_t, dq, dk, dv, dgate, dcum, fb, qg, kg, tm=256):
    s = u_b.shape[0]
    nb = s // tm
    d = h_t.shape[0]

    def body(ub_ref, ht_ref, dq_ref, dk_ref, dv_ref, dg_ref, dc_ref, fb_ref, qg_ref, kg_ref,
             du_ref, dwb_ref, dqg_ref, dkg_ref, dfb_ref, carry):
        i = pl.program_id(0)

        @pl.when(i == 0)
        def _():
            carry[...] = jnp.zeros_like(carry)
            dwb_ref[...] = jnp.zeros_like(dwb_ref)
            dqg_ref[...] = jnp.zeros_like(dqg_ref)
            dkg_ref[...] = jnp.zeros_like(dkg_ref)
            dfb_ref[...] = jnp.zeros_like(dfb_ref)

        bd = _head_ones()
        for lo, g_ref, d_ref, dgain_ref in ((0, qg_ref, dq_ref, dqg_ref), (512, kg_ref, dk_ref, dkg_ref)):
            gain = g_ref[...]
            xh, rinv, _ = _head_rms(ub_ref[:, lo:lo + 512], gain, bd)
            dn = d_ref[...]
            dgain_ref[...] += _colsum(dn * xh)
            dxh = dn * gain
            du_ref[:, lo:lo + 512] = rinv * (dxh - xh * (_head_sum(dxh * xh, bd) * (1.0 / HEAD)))
        du_ref[:, 1024:1536] = dv_ref[...]
        du_ref[:, 1536:2048] = dg_ref[...]
        lane = lax.broadcasted_iota(jnp.int32, (1, LANES), 1)
        dc = dc_ref[...]
        dlogf = _exact_dot(dc, _tri(tm, False), ones_first=True) + carry[...]
        carry[...] += _colsum(dc)
        fl = ub_ref[:, 2048:2176] + fb_ref[...]
        dfl = jnp.where(lane < N_HEADS, dlogf * (1.0 - _sigmoid(fl)), 0.0)
        du_ref[:, 2048:2176] = dfl
        dfb_ref[...] += _colsum(dfl)
        dwb_ref[...] += jnp.dot(ht_ref[...], du_ref[...].astype(BF16), preferred_element_type=F32)

    rev = lambda w: pl.BlockSpec((tm, w), lambda i: (nb - 1 - i, 0))
    vec = lambda w: pl.BlockSpec((1, w), lambda i: (0, 0))
    return pl.pallas_call(
        body, name="fox_prep_bwd", grid=(nb,),
        in_specs=[rev(SEC), pl.BlockSpec((d, tm), lambda i: (0, nb - 1 - i))] + [rev(D_HALF)] * 4
                 + [rev(LANES), vec(LANES), vec(D_HALF), vec(D_HALF)],
        out_specs=[rev(SEC), pl.BlockSpec((d, SEC), lambda i: (0, 0)), vec(D_HALF), vec(D_HALF), vec(LANES)],
        out_shape=[jax.ShapeDtypeStruct((s, SEC), F32), jax.ShapeDtypeStruct((d, SEC), F32),
                   jax.ShapeDtypeStruct((1, D_HALF), F32), jax.ShapeDtypeStruct((1, D_HALF), F32),
                   jax.ShapeDtypeStruct((1, LANES), F32)],
        scratch_shapes=[pltpu.VMEM((1, LANES), F32)],
        compiler_params=_params("arbitrary"),
    )(u_b, h_t, dq, dk, dv, dgate, dcum, fb, qg, kg)


def _merge(y, r, k, v, gate_a, o, u_b, h, x, tgt, w_g, wa, wb, wo, fg, lw, lb, rk, tm=256):
    s, d = x.shape

    def body(y_ref, r_ref, k_ref, v_ref, ga_ref, o_ref, gb_ref, h_ref, x_ref, t_ref, wg_ref, wa_ref, wb_ref, wo_ref,
             fg_ref, lw_ref, lb_ref, rk_ref,
             dx2_ref, dy_ref, drb_ref, dkb_ref, dvb_ref, dga_ref, do_ref, dgb_ref, dug_ref,
             dwa_ref, dwb_ref, dwo_ref, dfg_ref, loss_ref, dlw_ref, dlb_ref, drk_ref):
        i = pl.program_id(0)

        @pl.when(i == 0)
        def _():
            for ref in (dwa_ref, dwb_ref, dwo_ref, dfg_ref, loss_ref, dlw_ref, dlb_ref, drk_ref):
                ref[...] = jnp.zeros_like(ref)

        bd = _head_ones()
        wa_v, wb_v, wo_v, fg_v = wa_ref[...], wb_ref[...], wo_ref[...], fg_ref[...]
        rv, kv, vv, ga, lw_v, rk_v = r_ref[...], k_ref[...], v_ref[...], ga_ref[...], lw_ref[...], rk_ref[...]
        yn, rstd, rkk, sga, pre = _rwkv_post_math(y_ref[...], rv, kv, vv, ga, lw_v, lb_ref[...], rk_v, bd)
        silu_a = ga * sga
        gb, ov = gb_ref[...], o_ref[...]
        sgb = _sigmoid(gb)
        silu_b = gb * sgb
        ma = (pre * silu_a).astype(BF16)
        mb = (ov * silu_b).astype(BF16)
        ya = jnp.dot(ma, wa_v, preferred_element_type=F32)
        yb = jnp.dot(mb, wb_v, preferred_element_type=F32)
        ug = jnp.dot(h_ref[...], wg_ref[...], preferred_element_type=F32)
        sa = _sigmoid(ug[:, 0:d])
        sb = _sigmoid(ug[:, d:2 * d])
        merged = (sa * ya + sb * yb).astype(BF16)
        x2 = x_ref[...] + jnp.dot(merged, wo_v, preferred_element_type=F32)
        r2 = lax.rsqrt(jnp.mean(x2 * x2, axis=-1, keepdims=True) + RMS_EPS)
        x2h = x2 * r2
        err = x2h * fg_v - t_ref[...]
        loss_ref[...] += _colsum(err * err)
        dyo = err * (1.0 / d)
        dfg_ref[...] += _colsum(dyo * x2h)
        dx2h = dyo * fg_v
        dx2 = r2 * (dx2h - x2h * jnp.mean(dx2h * x2h, axis=-1, keepdims=True))
        dx2_ref[...] = dx2
        dx2b = dx2.astype(BF16)
        dmerged = _dot_nt(dx2b, wo_v)
        dwo_ref[...] += _dot_tn(merged, dx2b)
        dya = dmerged * sa
        dyb = dmerged * sb
        dug_ref[:, 0:d] = dya * ya * (1.0 - sa)
        dug_ref[:, d:2 * d] = dyb * yb * (1.0 - sb)
        dyab = dya.astype(BF16)
        dybb = dyb.astype(BF16)
        dwa_ref[...] += _dot_tn(ma, dyab)
        dwb_ref[...] += _dot_tn(mb, dybb)
        dmb = _dot_nt(dybb, wb_v)
        do_ref[...] = (dmb * silu_b).astype(BF16)
        dgb_ref[...] = dmb * ov * (sgb * (1.0 + gb * (1.0 - sgb)))
        dma = _dot_nt(dyab, wa_v)
        dga_ref[...] = dma * pre * (sga * (1.0 + ga * (1.0 - sga)))
        dpre = dma * silu_a
        dlw_ref[...] += _colsum(dpre * yn)
        dlb_ref[...] += _colsum(dpre)
        dyn = dpre * lw_v
        m1 = _head_sum(dyn, bd) * (1.0 / HEAD)
        m2 = _head_sum(dyn * yn, bd) * (1.0 / HEAD)
        dy_ref[...] = rstd * (dyn - m1 - yn * m2)
        dvb_ref[...] = dpre * rkk
        drkk = _head_sum(dpre * vv, bd)
        drb_ref[...] = drkk * kv * rk_v
        dkb_ref[...] = drkk * rv * rk_v
        drk_ref[...] += _colsum(drkk * rv * kv)

    row = lambda w: pl.BlockSpec((tm, w), lambda i: (i, 0))
    full = lambda a: pl.BlockSpec(a.shape, lambda i: (0, 0))
    once = lambda a: pl.BlockSpec(a.shape, lambda i: (0, 0), pipeline_mode=pl.Buffered(1))
    half = jax.ShapeDtypeStruct((s, D_HALF), F32)
    fshape = lambda a: jax.ShapeDtypeStruct(a.shape, F32)
    return pl.pallas_call(
        body, name="merge_fwd_bwd", grid=(s // tm,),
        in_specs=[row(D_HALF)] * 6 + [pl.BlockSpec((tm, D_HALF), lambda i: (i, 3)), row(d), row(d), row(d),
                                      once(w_g), once(wa), once(wb), once(wo), full(fg), full(lw), full(lb), full(rk)],
        out_specs=[row(d)] + [row(D_HALF)] * 7 + [row(GATE_COLS), full(wa), full(wb), full(wo), full(fg), full(fg),
                                                   full(lw), full(lb), full(rk)],
        out_shape=[jax.ShapeDtypeStruct((s, d), F32)] + [half] * 5 + [jax.ShapeDtypeStruct((s, D_HALF), BF16), half,
                                                                    jax.ShapeDtypeStruct((s, GATE_COLS), F32),
                                                                    fshape(wa), fshape(wb), fshape(wo), fshape(fg),
                                                                    fshape(fg), fshape(lw), fshape(lb), fshape(rk)],
        compiler_params=_params("arbitrary"),
    )(y, r, k, v, gate_a, o, u_b, h, x, tgt, w_g, wa, wb, wo, fg, lw, lb, rk)


def _lora_weight(w_up, a_up):
    z = jnp.zeros((LORA, D_HALF), w_up.dtype)
    return jnp.concatenate([jnp.concatenate([w_up, z], axis=1), jnp.concatenate([z, a_up], axis=1)], axis=0)


def _device_grads(x, tgt, p, w_a, w_up, a_up, late_weights, fwd_exchange=None, bwd_exchange=None, tail_exchange=None):
    wl = _lora_weight(w_up, a_up)
    rk = p["r_k"].reshape(1, D_HALF)
    fb = jnp.pad(p["f_bias"], ((0, 0), (0, LANES - N_HEADS)))
    qg = jnp.tile(p["q_norm_g"], (1, N_HEADS))
    kg = jnp.tile(p["k_norm_g"], (1, N_HEADS))
    fg = p["final_norm_g"].reshape(1, D_MODEL)
    mixer = (p["shift_mu"], wl, p["w0"], p["a0"], p["k_k"], p["k_a"])

    h, u_a, r, dec, k, v, av, bv, gate_a = _rwkv_front(x, p["norm_g"], w_a, *mixer)
    y, st, arrived = _wkv_fwd(r, dec, k, av, bv, v, fwd_exchange)

    w_b, w_g, w_out_a, w_out_b, w_out = late_weights(arrived)
    u_b, q, kn, vb, cc, cr = _fox_front(h, w_b, fb, qg, kg)
    o, lse = _attn_fwd(q, kn, vb, cc, cr)

    (dx2, dy, dr_b, dk_b, dv_b, dgate_a, do, dgate_b, du_g, dwa, dwb, dwo, dfg, loss_vec, dlw, dlb, drk) = _merge(
        y, r, k, v, gate_a, o, u_b, h, x, tgt, w_g, w_out_a, w_out_b, w_out, fg, p["lnx_w"], p["lnx_b"], rk)

    dd = _attn_bwd_rowdot(q, kn, vb, do, lse, cc, cr)
    dq, dk_att, dv_att, dcr = _attn_bwd(q, kn, vb, do, lse, dd, cc, cr)
    dcum = jnp.pad(dcr.T, ((0, 0), (0, LANES - N_HEADS)))
    h_t = h.T
    du_b, dw_b, dqg, dkg, dfb = _fox_prep_bwd(u_b, h_t, dq, dk_att, dv_att, dgate_b, dcum, fb, qg, kg)
    dw_g = _matmul_tn_acc(h_t, du_g, "dw_gate")

    scan_grads, sent = _wkv_bwd(r, dec, k, av, bv, v, dy, st,
                                bwd_exchange(dw_b, dw_g, dwa, dwb, dwo) if bwd_exchange else None)
    du_a, dw_a, dmu, dwl, dw0, da0, dkkw, dkaw = _rwkv_prep_bwd(
        u_a, h_t, (*scan_grads, dr_b, dk_b, dv_b, dgate_a), *mixer)
    dw_up, da_up = dwl[:LORA, :D_HALF], dwl[LORA:, D_HALF:]
    sent_last = _run_on_sequencer(tail_exchange(dw_a, dw_up, da_up), "scatter_tail", 1) if tail_exchange else []
    grad_x, dnorm_g, _ = _inproj_bwd(du_a, du_b, du_g, w_a, w_b, w_g, x, dx2, p["norm_g"])

    grads = dict(
        norm_g=dnorm_g, w_in=(dw_a, dw_b, dw_g), shift_mu=dmu,
        w_lora_up=dw_up, w0=dw0, a_lora_up=da_up, a0=da0, k_k=dkkw, k_a=dkaw,
        r_k=drk.reshape(1, N_HEADS, HEAD), lnx_w=dlw, lnx_b=dlb, f_bias=dfb[:, :N_HEADS],
        q_norm_g=dqg.reshape(N_HEADS, HEAD).sum(axis=0, keepdims=True),
        k_norm_g=dkg.reshape(N_HEADS, HEAD).sum(axis=0, keepdims=True),
        w_out_a=dwa, w_out_b=dwb, w_out=dwo, final_norm_g=dfg.reshape(D_MODEL))
    return loss_vec, grad_x, grads, sent, sent_last


CHIP_FLIPS = ((1, 0), (0, 1), (1, 1))
ANY = pl.BlockSpec(memory_space=pl.ANY)


def _position():
    return lax.axis_index("x"), lax.axis_index("y"), lax.axis_index("c")


def _flip(v, f):
    return 1 - v if f else v


def _both(a, b):
    if a is None:
        return b
    return a if b is None else jnp.logical_and(a, b)


def _when(cond, fn):
    if cond is None:
        fn()
    else:
        pl.when(cond)(fn)


class _Moves:
    def __init__(self, send_sems, recv_sems, local_sems):
        self.send_sems, self.recv_sems, self.local_sems = send_sems, recv_sems, local_sems
        self.remote, self.local = [], []

    def send(self, src, dst, peer, landing, send_if=None, recv_if=None, first=False):
        k = len(self.remote)
        sems = dict(send_sem=self.send_sems.at[k], recv_sem=self.recv_sems.at[k], device_id=peer, device_id_type=MESH)
        out = pltpu.make_async_remote_copy(src_ref=src, dst_ref=dst, **sems)
        arrival = pltpu.make_async_remote_copy(src_ref=src, dst_ref=landing, **sems)
        self.remote.append((out, arrival, send_if, recv_if, first))

    def copy(self, src, dst, cond=None):
        cp = pltpu.make_async_copy(src, dst, self.local_sems.at[len(self.local)])
        self.local.append((cp, cond))

    def start(self, also=None):
        for cp, cond in self.local:
            _when(_both(also, cond), cp.start)
        for out, _, send_if, _, _ in self.remote:
            _when(_both(also, send_if), out.start)

    def wait_arrivals(self, also=None, first=None):
        for _, arrival, _, recv_if, is_first in self.remote:
            if first is None or first == is_first:
                _when(_both(also, recv_if), arrival.wait_recv)

    def wait_sent(self, also=None):
        for out, _, send_if, _, _ in self.remote:
            _when(_both(also, send_if), out.wait_send)
        for cp, cond in self.local:
            _when(_both(also, cond), cp.wait)

    def wait(self, also=None):
        self.wait_arrivals(also)
        self.wait_sent(also)


class _Exchange:
    def __init__(self, operands, out_shapes, n_remote, n_local, build, relays=None, in_place=()):
        self.operands, self.out_shapes = list(operands), list(out_shapes)
        self.n_remote, self.n_local, self.build = n_remote, n_local, build
        self.relays, self.in_place = relays, in_place

    def scratch(self):
        return [pltpu.SemaphoreType.DMA((self.n_remote,)), pltpu.SemaphoreType.DMA((self.n_remote,)),
                pltpu.SemaphoreType.DMA((max(self.n_local, 1),))]

    def moves(self, in_refs, out_refs, sems):
        mv = _Moves(*sems)
        self.build(mv, in_refs, out_refs)
        return mv


def _run_on_sequencer(exchange, name, collective_id):
    ins = [jax.new_ref(a, memory_space=pltpu.MemorySpace.HBM) for a in exchange.operands]
    outs = [ins[i] if i in exchange.in_place else jax.empty_ref(s, memory_space=pltpu.MemorySpace.HBM)
            for i, s in enumerate(exchange.out_shapes)]
    relay_scratch = [pltpu.SemaphoreType.DMA((n,)) for n, _ in exchange.relays or () for _ in range(2)]

    def launch(*sems):
        x, y, c = _position()
        peers = [(_flip(x, fx), _flip(y, fy), c) for fx, fy in CHIP_FLIPS] + ([(x, y, 1 - c)] if exchange.relays else [])
        barrier = pltpu.get_barrier_semaphore()
        for peer in peers:
            pl.semaphore_signal(barrier, inc=1, device_id=peer, device_id_type=MESH)
        pl.semaphore_wait(barrier, len(peers))
        moves = exchange.moves(ins, outs, sems[:3])
        moves.start()
        if exchange.relays:
            (_, forward), (_, to_sibling) = exchange.relays
            onward, passed = _Moves(sems[3], sems[4], None), _Moves(sems[5], sems[6], None)
            forward(onward, ins, outs)
            to_sibling(passed, ins, outs)
            moves.wait_arrivals(first=True)
            onward.start()
            moves.wait_arrivals(first=False)
            onward.wait_arrivals()
            passed.start()
            passed.wait()
            onward.wait_sent()
        else:
            moves.wait_arrivals()
        moves.wait_sent()

    pl.kernel(launch, mesh=plsc.ScalarSubcoreMesh(axis_name="sequencer", num_cores=1), name=name,
              scratch_types=tuple(exchange.scratch() + relay_scratch),
              compiler_params=pltpu.CompilerParams(collective_id=collective_id))()
    return [o[...] for o in outs]


def _row_major_copy(a, name):
    r, c = a.shape
    tr = _row_tile(r)

    def body(a_ref, o_ref):
        o_ref[...] = a_ref[...]

    blk = pl.BlockSpec((tr, c), lambda i: (i, 0))
    return pl.pallas_call(body, name=name, grid=(r // tr,), in_specs=[blk], out_specs=blk,
                          out_shape=jax.ShapeDtypeStruct(a.shape, a.dtype), compiler_params=_params("parallel"))(a)


def _is_chip(x, y, chip):
    return jnp.logical_and(x == chip // 2, y == chip % 2)


def _gather_exchange(from_chip, from_all, split=()):
    n1, n2 = len(from_chip), len(from_all)
    near = CHIP_FLIPS[:2]

    def quarters(t, c, first, count=1):
        n = from_chip[t][1].shape[0] // 4
        return pl.ds((2 * c + first) * n, count * n)

    def build(mv, ins, outs):
        x, y, c = _position()
        me = 2 * x + y
        for t, (chip, _) in enumerate(from_chip):
            if t not in split:
                mv.copy(ins[t], outs[t], cond=_is_chip(x, y, chip))
        for t in range(n2):
            mv.copy(ins[n1 + t], outs[n1 + t].at[me])
        for t in split:
            for first in (True, False):
                for f, (fx, fy) in enumerate(near):
                    px, py = _flip(x, fx), _flip(y, fy)
                    part = quarters(t, c, f if first else 1 - f)
                    mv.send(ins[t].at[part], outs[t].at[part], (px, py, c), landing=outs[t].at[part], first=first,
                            send_if=_is_chip(x, y, from_chip[t][0]), recv_if=_is_chip(px, py, from_chip[t][0]))
        for fx, fy in CHIP_FLIPS:
            px, py = _flip(x, fx), _flip(y, fy)
            peer = (px, py, c)
            for t, (chip, _) in enumerate(from_chip):
                if t not in split:
                    mv.send(ins[t], outs[t], peer, landing=outs[t],
                            send_if=_is_chip(x, y, chip), recv_if=_is_chip(px, py, chip))
            for t in range(n2):
                mv.send(ins[n1 + t], outs[n1 + t].at[me], peer, landing=outs[n1 + t].at[2 * px + py])

    def forward(mv, ins, outs):
        x, y, c = _position()
        for t in split:
            chip = from_chip[t][0]
            for f, (fx, fy) in enumerate(near):
                gx, gy = near[1 - f]
                part = quarters(t, c, f)
                mv.send(outs[t].at[part], outs[t].at[part], (_flip(x, gx), _flip(y, gy), c), landing=outs[t].at[part],
                        send_if=_is_chip(_flip(x, fx), _flip(y, fy), chip), recv_if=_is_chip(1 - x, 1 - y, chip))

    def to_sibling(mv, ins, outs):
        x, y, c = _position()
        for t in split:
            came = jnp.logical_not(_is_chip(x, y, from_chip[t][0]))
            mv.send(outs[t].at[quarters(t, c, 0, 2)], outs[t].at[quarters(t, c, 0, 2)], (x, y, 1 - c),
                    landing=outs[t].at[quarters(t, 1 - c, 0, 2)], send_if=came, recv_if=came)

    arrays = [a for _, a in from_chip] + list(from_all)
    shapes = [jax.ShapeDtypeStruct(a.shape, a.dtype) for _, a in from_chip]
    shapes += [jax.ShapeDtypeStruct((N_CHIPS,) + a.shape, a.dtype) for a in from_all]
    n_remote = len(CHIP_FLIPS) * (n1 - len(split) + n2) + 2 * len(near) * len(split)
    relays = ((len(near) * len(split), forward), (len(split), to_sibling)) if split else None
    return _Exchange(arrays, shapes, n_remote, n1 + n2, build, relays, in_place=split)


def _scatter_exchange(to_chip, to_all):
    n1, n2 = len(to_chip), len(to_all)

    def build(mv, ins, outs):
        x, y, c = _position()
        for f, (fx, fy) in enumerate(CHIP_FLIPS):
            px, py = _flip(x, fx), _flip(y, fy)
            peer = (px, py, c)
            for t, (chip, _) in enumerate(to_chip):
                mv.send(ins[t], outs[t].at[f], peer, landing=outs[t].at[f],
                        send_if=_is_chip(px, py, chip), recv_if=_is_chip(x, y, chip))
            for t in range(n2):
                mv.send(ins[n1 + t].at[2 * px + py], outs[n1 + t].at[f], peer, landing=outs[n1 + t].at[f])

    arrays = [a for _, a in to_chip] + list(to_all)
    shapes = [jax.ShapeDtypeStruct((len(CHIP_FLIPS),) + a.shape, a.dtype) for _, a in to_chip]
    shapes += [jax.ShapeDtypeStruct((len(CHIP_FLIPS),) + a.shape[1:], a.dtype) for a in to_all]
    return _Exchange(arrays, shapes, len(CHIP_FLIPS) * (n1 + n2), 0, build)


def _swap_sibling(tensors, name):
    n = len(tensors)

    def body(*refs):
        ins, outs = refs[:n], refs[n:2 * n]
        send_sems, recv_sems = refs[2 * n:]
        x, y, c = _position()
        copies = [pltpu.make_async_remote_copy(
            src_ref=ins[t], dst_ref=outs[t], send_sem=send_sems.at[t], recv_sem=recv_sems.at[t],
            device_id=(x, y, 1 - c), device_id_type=MESH) for t in range(n)]
        for cp in copies:
            cp.start()
        for cp in copies:
            cp.wait_recv()
        for cp in copies:
            cp.wait_send()

    return pl.pallas_call(
        body, name=name, in_specs=[ANY] * n, out_specs=[ANY] * n,
        out_shape=[jax.ShapeDtypeStruct(a.shape, a.dtype) for a in tensors],
        scratch_shapes=[pltpu.SemaphoreType.DMA((n,)), pltpu.SemaphoreType.DMA((n,))],
        compiler_params=pltpu.CompilerParams(has_side_effects=True),
    )(*tensors)


def _allreduce_small(slab):
    stages = 3

    def body(x_ref, o_ref, buf, send_sems, recv_sems):
        x, y, c = _position()
        peers = ((1 - x, y, c), (x, 1 - y, c), (x, y, 1 - c))
        o_ref[...] = x_ref[...]
        for k, peer in enumerate(peers):
            cp = pltpu.make_async_remote_copy(src_ref=o_ref, dst_ref=buf.at[k], send_sem=send_sems.at[k],
                                              recv_sem=recv_sems.at[k], device_id=peer, device_id_type=MESH)
            cp.start()
            cp.wait()
            o_ref[...] = o_ref[...] + buf[k]

    return pl.pallas_call(
        body, name="allreduce_small",
        in_specs=[pl.BlockSpec(memory_space=pltpu.VMEM)], out_specs=pl.BlockSpec(memory_space=pltpu.VMEM),
        out_shape=jax.ShapeDtypeStruct(slab.shape, slab.dtype),
        scratch_shapes=[pltpu.VMEM((stages,) + slab.shape, slab.dtype),
                        pltpu.SemaphoreType.DMA((stages,)), pltpu.SemaphoreType.DMA((stages,))],
        compiler_params=pltpu.CompilerParams(has_side_effects=True),
    )(slab)


def _row_tile(r):
    return min(r, 256)


def _sum4(stack, recv, me):
    _, r, c = stack.shape
    tr = _row_tile(r)

    def body(me_ref, own_ref, recv_ref, o_ref):
        o_ref[...] = (((own_ref[...] + recv_ref[0].astype(F32)) + recv_ref[1].astype(F32))
                      + recv_ref[2].astype(F32))

    return pl.pallas_call(
        body, name="sum_partials",
        grid_spec=pltpu.PrefetchScalarGridSpec(
            num_scalar_prefetch=1, grid=(r // tr,),
            in_specs=[pl.BlockSpec((None, tr, c), lambda i, me_ref: (me_ref[0], i, 0)),
                      pl.BlockSpec((len(CHIP_FLIPS), tr, c), lambda i, me_ref: (0, i, 0))],
            out_specs=pl.BlockSpec((tr, c), lambda i, me_ref: (i, 0))),
        out_shape=jax.ShapeDtypeStruct((r, c), F32), compiler_params=_params("parallel"),
    )(me, stack, recv)


def _sum_block(own, recv):
    r, c = own.shape
    tr = _row_tile(r)

    def body(own_ref, recv_ref, o_ref):
        o_ref[...] = (((own_ref[...] + recv_ref[0].astype(F32)) + recv_ref[1].astype(F32))
                      + recv_ref[2].astype(F32))

    return pl.pallas_call(
        body, name="sum_block", grid=(r // tr,),
        in_specs=[pl.BlockSpec((tr, c), lambda i: (i, 0)), pl.BlockSpec((len(CHIP_FLIPS), tr, c), lambda i: (0, i, 0))],
        out_specs=pl.BlockSpec((tr, c), lambda i: (i, 0)),
        out_shape=jax.ShapeDtypeStruct((r, c), F32), compiler_params=_params("parallel"),
    )(own, recv)


def _adamw_math(w, g, m, v):
    m = ADAM_B1 * m + (1.0 - ADAM_B1) * g
    v = ADAM_B2 * v + (1.0 - ADAM_B2) * (g * g)
    m_hat = m / (1.0 - ADAM_B1 ** ADAM_STEP)
    v_hat = v / (1.0 - ADAM_B2 ** ADAM_STEP)
    delta = -ADAM_LR * (m_hat / (jnp.sqrt(v_hat) + ADAM_EPS) + ADAM_WD * w)
    return delta, m, v


def _adamw(w, m, v, g_parts, name):
    r, c = w.shape
    tr = _row_tile(r)
    n = len(g_parts)

    def body(*refs):
        w_ref, m_ref, v_ref = refs[:3]
        g_refs = refs[3:3 + n]
        g_out, d_out, m_out, v_out = refs[3 + n:]
        g = g_refs[0][...]
        for ref in g_refs[1:]:
            g = g + ref[...]
        g_out[...] = g
        d_out[...], m_out[...], v_out[...] = _adamw_math(w_ref[...], g, m_ref[...], v_ref[...])

    blk = pl.BlockSpec((tr, c), lambda i: (i, 0))
    return pl.pallas_call(
        body, name=name, grid=(r // tr,), in_specs=[blk] * (3 + n), out_specs=[blk] * 4,
        out_shape=[jax.ShapeDtypeStruct((r, c), F32)] * 4, compiler_params=_params("parallel"),
    )(w, m, v, *g_parts)


def _adamw_small(total, w, m, v):
    sizes = [w[n].size for n in SMALL]
    flat = lambda d: [d[n].reshape(1, -1) for n in SMALL]
    k = len(SMALL)

    def body(*refs):
        total_ref, w_refs, m_refs, v_refs = refs[0], refs[1:1 + k], refs[1 + k:1 + 2 * k], refs[1 + 2 * k:1 + 3 * k]
        outs = refs[1 + 3 * k:]
        for i, size in enumerate(sizes):
            g = total_ref[i:i + 1, 0:size]
            outs[i][...] = g
            outs[k + i][...], outs[2 * k + i][...], outs[3 * k + i][...] = _adamw_math(
                w_refs[i][...], g, m_refs[i][...], v_refs[i][...])

    res = pl.pallas_call(
        body, name="adamw_small", out_shape=[jax.ShapeDtypeStruct((1, size), F32) for size in sizes] * 4,
        compiler_params=_params(),
    )(total, *flat(w), *flat(m), *flat(v))
    return [{n: res[j * k + i].reshape(w[n].shape) for i, n in enumerate(SMALL)} for j in range(4)]


SHARDED = ("w_in", "w_lora_up", "a_lora_up", "w_out_a", "w_out_b", "w_out")
ROW_SHARDED = ("w_out",)
SMALL = ("norm_g", "shift_mu", "w0", "a0", "k_k", "k_a", "r_k", "lnx_w", "lnx_b", "f_bias", "q_norm_g", "k_norm_g",
         "final_norm_g")
WEIGHTS = ("norm_g", "w_in", "shift_mu", "w_lora_up", "w0", "a_lora_up", "a0", "k_k", "k_a", "r_k", "lnx_w", "lnx_b",
           "f_bias", "q_norm_g", "k_norm_g", "w_out_a", "w_out_b", "w_out", "final_norm_g")
SLAB_ROWS = 16
SLAB_COLS = SEC


def _to_slab(named, extra=None):
    rows = [jnp.pad(named[n].reshape(1, -1), ((0, 0), (0, SLAB_COLS - named[n].size))) for n in SMALL]
    if extra is not None:
        rows.append(jnp.pad(extra.reshape(1, -1), ((0, 0), (0, SLAB_COLS - extra.size))))
    rows.append(jnp.zeros((SLAB_ROWS - len(rows), SLAB_COLS), F32))
    return jnp.concatenate(rows, axis=0)


def _by_chip(g, name):
    if name in ROW_SHARDED:
        return g.reshape(N_CHIPS, g.shape[0] // N_CHIPS, g.shape[1])
    r, c = g.shape
    return g.reshape(r, N_CHIPS, c // N_CHIPS).transpose(1, 0, 2)


def _from_chips(stack, name):
    if name in ROW_SHARDED:
        return stack.reshape(-1, stack.shape[2])
    _, r, c = stack.shape
    return stack.transpose(1, 0, 2).reshape(r, N_CHIPS * c)


def kernel(x, norm_g, w_in, shift_mu, w_lora_up, w0, a_lora_up, a0, k_k, k_a, r_k, lnx_w, lnx_b, f_bias, q_norm_g, k_norm_g, w_out_a, w_out_b, w_out, final_norm_g, loss_target, m_norm_g, m_w_in, m_shift_mu, m_w_lora_up, m_w0, m_a_lora_up, m_a0, m_k_k, m_k_a, m_r_k, m_lnx_w, m_lnx_b, m_f_bias, m_q_norm_g, m_k_norm_g, m_w_out_a, m_w_out_b, m_w_out, m_final_norm_g, v_norm_g, v_w_in, v_shift_mu, v_w_lora_up, v_w0, v_a_lora_up, v_a0, v_k_k, v_k_a, v_r_k, v_lnx_w, v_lnx_b, v_f_bias, v_q_norm_g, v_k_norm_g, v_w_out_a, v_w_out_b, v_w_out, v_final_norm_g):
    w = dict(norm_g=norm_g, w_in=w_in, shift_mu=shift_mu, w_lora_up=w_lora_up, w0=w0, a_lora_up=a_lora_up, a0=a0,
             k_k=k_k, k_a=k_a, r_k=r_k, lnx_w=lnx_w, lnx_b=lnx_b, f_bias=f_bias, q_norm_g=q_norm_g,
             k_norm_g=k_norm_g, w_out_a=w_out_a, w_out_b=w_out_b, w_out=w_out, final_norm_g=final_norm_g)
    m = dict(norm_g=m_norm_g, w_in=m_w_in, shift_mu=m_shift_mu, w_lora_up=m_w_lora_up, w0=m_w0,
             a_lora_up=m_a_lora_up, a0=m_a0, k_k=m_k_k, k_a=m_k_a, r_k=m_r_k, lnx_w=m_lnx_w, lnx_b=m_lnx_b,
             f_bias=m_f_bias, q_norm_g=m_q_norm_g, k_norm_g=m_k_norm_g, w_out_a=m_w_out_a, w_out_b=m_w_out_b,
             w_out=m_w_out, final_norm_g=m_final_norm_g)
    v = dict(norm_g=v_norm_g, w_in=v_w_in, shift_mu=v_shift_mu, w_lora_up=v_w_lora_up, w0=v_w0,
             a_lora_up=v_a_lora_up, a0=v_a0, k_k=v_k_k, k_a=v_k_a, r_k=v_r_k, lnx_w=v_lnx_w, lnx_b=v_lnx_b,
             f_bias=v_f_bias, q_norm_g=v_q_norm_g, k_norm_g=v_k_norm_g, w_out_a=v_w_out_a, w_out_b=v_w_out_b,
             w_out=v_w_out, final_norm_g=v_final_norm_g)
    shapes = {n: w[n].shape for n in WEIGHTS}

    shard = {n: w[n][0].astype(BF16) for n in SHARDED}
    late = ("w_out_a", "w_out_b", "w_out")
    loras = ("w_lora_up", "a_lora_up")
    w_in_head, w_in_tail = shard["w_in"][:, :A_TAIL], shard["w_in"][:, A_TAIL:]
    shard0, shard1_head, up_stack, aup_stack = _run_on_sequencer(_gather_exchange(
        [(0, shard["w_in"]), (1, w_in_head)], [shard[n] for n in loras], split=(0, 1)), "gather_early", 2)
    moments = (_row_major_copy(m["w_in"][0], "m_w_in_rows"), _row_major_copy(v["w_in"][0], "v_w_in_rows"))
    shard0, moments = lax.optimization_barrier((shard0, moments))
    w_a = jnp.concatenate([shard0, shard1_head], axis=1)

    def late_weights(arrived):
        shard1_tail, shard2, shard3 = arrived[:3]
        w_b = jnp.concatenate([shard1_tail, shard2[:, :B_TAIL], jnp.zeros((D_MODEL, SEC - FOX_REAL), BF16)], axis=1)
        w_g = jnp.concatenate([shard2[:, B_TAIL:], shard3], axis=1)
        return (w_b, w_g, *[_from_chips(s, n) for n, s in zip(late, arrived[3:])])

    own = {}

    def bwd_exchange(dw_b, dw_g, dwa, dwb, dwo):
        own["tail1"] = dw_b[:, :B_HEAD]
        own["block2"] = jnp.concatenate([dw_b[:, B_HEAD:FOX_REAL], dw_g[:, :G_HEAD]], axis=1)
        own["block3"] = dw_g[:, G_HEAD:]
        own.update({n: _by_chip(g, n) for n, g in zip(late, (dwa, dwb, dwo))})
        return _scatter_exchange([(1, own["tail1"].astype(BF16)), (2, own["block2"].astype(BF16)),
                                  (3, own["block3"].astype(BF16))], [own[n].astype(BF16) for n in late])

    def tail_exchange(dw_a, dw_up, da_up):
        own["block0"], own["head1"] = dw_a[:, :SHARD_COLS], dw_a[:, SHARD_COLS:]
        own.update({n: _by_chip(g, n) for n, g in zip(loras, (dw_up, da_up))})
        return _scatter_exchange([(0, own["block0"].astype(BF16)), (1, own["head1"].astype(BF16))],
                                 [own[n].astype(BF16) for n in loras])

    small = {n: w[n] for n in SMALL}
    loss_vec, grad_x, grads, sent, sent_last = _device_grads(
        x[0], loss_target[0], small, w_a, _from_chips(up_stack, "w_lora_up"), _from_chips(aup_stack, "a_lora_up"),
        late_weights, _gather_exchange([(1, w_in_tail), (2, shard["w_in"]), (3, shard["w_in"])], [shard[n] for n in late]),
        bwd_exchange, tail_exchange)

    total = _allreduce_small(_to_slab(grads, extra=loss_vec))
    loss = (0.5 / D_MODEL) * jnp.sum(total[len(SMALL)])
    out_g, out_d, out_m, out_v = _adamw_small(total, w, m, v)

    xpos, ypos, _ = _position()
    me = (2 * xpos + ypos).astype(jnp.int32).reshape(1)
    core_sum = {n: _sum4(own[n], r, me) for n, r in zip(late, sent[3:])}
    theirs = dict(zip(late, _swap_sibling([core_sum[n] for n in late], "swap_sibling_early")))

    def update(n):
        m_n, v_n = moments if n == "w_in" else (m[n][0], v[n][0])
        g, d, m2, v2 = _adamw(w[n][0], m_n, v_n, [core_sum[n], theirs[n]], "adamw_" + n)
        out_g[n], out_d[n], out_m[n], out_v[n] = (a.reshape(shapes[n]) for a in (g, d, m2, v2))

    for n in late:
        update(n)
    done = (out_d["norm_g"], [out_d[n] for n in late])
    sent_last, (out_d["norm_g"], new_d) = lax.optimization_barrier((sent_last, done))
    out_d.update(zip(late, new_d))
    core_sum["w_in"] = lax.switch(me[0], [
        lambda: _sum_block(own["block0"], sent_last[0]),
        lambda: jnp.concatenate([_sum_block(own["head1"], sent_last[1]), _sum_block(own["tail1"], sent[0])], axis=1),
        lambda: _sum_block(own["block2"], sent[1]),
        lambda: _sum_block(own["block3"], sent[2])])
    core_sum.update({n: _sum4(own[n], r, me) for n, r in zip(loras, sent_last[2:])})
    rest = ("w_in",) + loras
    theirs.update(zip(rest, _swap_sibling([core_sum[n] for n in rest], "swap_sibling")))
    for n in rest:
        update(n)

    return (loss, grad_x.reshape(x.shape), *[out_g[n] for n in WEIGHTS], *[out_d[n] for n in WEIGHTS],
            *[out_m[n] for n in WEIGHTS], *[out_v[n] for n in WEIGHTS])
```

```python
import functools
import math

import jax
import jax.numpy as jnp
from jax import lax
from jax.experimental import pallas as pl
from jax.experimental.pallas import tpu as pltpu
from jax.experimental.pallas import tpu_sc as plsc

F32 = jnp.float32
BF16 = jnp.bfloat16

D_MODEL = 1024
D_HALF = 512
HEAD = 64
N_HEADS = 8
LORA = 64
RWKV_COLS = 2176
FOX_REAL = 2056
SEC = 2176
GATE_COLS = 2048
IN_COLS = 6280
N_CHIPS = 4
SHARD_COLS = IN_COLS // N_CHIPS
A_TAIL = RWKV_COLS - SHARD_COLS
B_HEAD = SHARD_COLS - A_TAIL
B_TAIL = FOX_REAL - B_HEAD
G_HEAD = SHARD_COLS - B_TAIL
RMS_EPS = 1e-6
LNX_EPS = 64e-5
ATT_SCALE = HEAD ** -0.5
NEG = -1e30

ADAM_LR = 0.001
ADAM_B1 = 0.9
ADAM_B2 = 0.999
ADAM_EPS = 1e-08
ADAM_WD = 0.01
ADAM_STEP = 10

LANES = 128
SUBLANES = 8
VMEM_LIMIT = 56 * 1024 * 1024
MESH = pl.DeviceIdType.MESH


def _params(*sem):
    return pltpu.CompilerParams(dimension_semantics=sem if sem else None, vmem_limit_bytes=VMEM_LIMIT)


def _sigmoid(x):
    return 1.0 / (1.0 + jnp.exp(-x))


def _log_sigmoid(x):
    return jnp.minimum(x, 0.0) - jnp.log(1.0 + jnp.exp(-jnp.abs(x)))


def _head_ones():
    r = lax.broadcasted_iota(jnp.int32, (LANES, LANES), 0) >> 6
    c = lax.broadcasted_iota(jnp.int32, (LANES, LANES), 1) >> 6
    return (r == c).astype(BF16)


def _split3(x):
    hi = x.astype(BF16)
    r1 = x - hi.astype(F32)
    mid = r1.astype(BF16)
    lo = (r1 - mid.astype(F32)).astype(BF16)
    return hi, mid, lo


def _exact_dot(x, ones_bf16, ones_first=False):
    out = None
    for piece in _split3(x):
        if ones_first:
            t = jnp.dot(ones_bf16, piece, preferred_element_type=F32)
        else:
            t = jnp.dot(piece, ones_bf16, preferred_element_type=F32)
        out = t if out is None else out + t
    return out


def _head_sum(x, bd):
    n = x.shape[1] // LANES
    parts = [_exact_dot(x[:, i * LANES:(i + 1) * LANES], bd) for i in range(n)]
    return parts[0] if n == 1 else jnp.concatenate(parts, axis=1)


def _dot_nt(a, b):
    return lax.dot_general(a, b, (((1,), (1,)), ((), ())), preferred_element_type=F32)


def _dot_tn(a, b):
    return lax.dot_general(a, b, (((0,), (0,)), ((), ())), preferred_element_type=F32)


def _colsum(x):
    return jnp.sum(x, axis=0, keepdims=True)


def _matmul_tn_acc(at, b, name, tk=512):
    m, k = at.shape
    n = b.shape[1]

    def body(a_ref, b_ref, o_ref):
        j = pl.program_id(0)

        @pl.when(j == 0)
        def _():
            o_ref[...] = jnp.zeros_like(o_ref)

        o_ref[...] += jnp.dot(a_ref[...], b_ref[...].astype(BF16), preferred_element_type=F32)

    return pl.pallas_call(
        body, name=name, grid=(k // tk,),
        in_specs=[pl.BlockSpec((m, tk), lambda j: (0, j)), pl.BlockSpec((tk, n), lambda j: (j, 0))],
        out_specs=pl.BlockSpec((m, n), lambda j: (0, 0)),
        out_shape=jax.ShapeDtypeStruct((m, n), F32), compiler_params=_params("arbitrary"),
    )(at, b)


def _inproj_bwd(du_a, du_b, du_g, w_a, w_b, w_g, x, dx2, g, exchange=None, tm=256):
    s, d = x.shape
    nb = s // tm

    def body(*refs):
        ((da_ref, db_ref, dg_ref, wa_ref, wb_ref, wg_ref, x_ref, dx2_ref, g_ref), (gx_ref, gg_ref), _,
         moves) = _split_refs(refs, 9, 2, exchange)
        i = pl.program_id(0)
        if moves:
            moves.start(also=(i == 0))

        @pl.when(i == 0)
        def _():
            gg_ref[...] = jnp.zeros_like(gg_ref)

        dh = _dot_nt(da_ref[...].astype(BF16), wa_ref[...])
        dh += _dot_nt(db_ref[...].astype(BF16), wb_ref[...])
        dh += _dot_nt(dg_ref[...].astype(BF16), wg_ref[...])
        xv = x_ref[...]
        r = lax.rsqrt(jnp.mean(xv * xv, axis=-1, keepdims=True) + RMS_EPS)
        xh = xv * r
        gg_ref[...] += _colsum(dh * xh)
        dxh = dh * g_ref[...]
        gx_ref[...] = dx2_ref[...] + r * (dxh - xh * jnp.mean(dxh * xh, axis=-1, keepdims=True))
        if moves:
            moves.wait(also=(i == nb - 1))

    row = lambda w: pl.BlockSpec((tm, w), lambda i: (i, 0))
    full = lambda a: pl.BlockSpec(a.shape, lambda i: (0, 0))
    ex_in = exchange.operands if exchange else []
    ex_out = exchange.out_shapes if exchange else []
    res = pl.pallas_call(
        body, name="inproj_bwd", grid=(nb,),
        in_specs=[row(SEC), row(SEC), row(GATE_COLS), full(w_a), full(w_b), full(w_g), row(d), row(d), full(g)]
                 + [ANY] * len(ex_in),
        out_specs=[row(d), pl.BlockSpec((1, d), lambda i: (0, 0))] + [ANY] * len(ex_out),
        out_shape=[jax.ShapeDtypeStruct((s, d), F32), jax.ShapeDtypeStruct((1, d), F32)] + ex_out,
        scratch_shapes=exchange.scratch() if exchange else [],
        compiler_params=_params("arbitrary"),
    )(du_a, du_b, du_g, w_a, w_b, w_g, x, dx2, g, *ex_in)
    return res[0], res[1], list(res[2:])


def _rwkv_elementwise(ua, prev_row, first, mu, wl, w0, a0, kkw, kaw, bd):
    tm = ua.shape[0]
    rows = lax.broadcasted_iota(jnp.int32, (tm, 1), 0)
    prev = jnp.where(first, jnp.zeros_like(prev_row), prev_row)
    shifted = jnp.where(rows == 0, prev, pltpu.roll(ua, 1, 0))
    delta = shifted - ua
    us = ua + delta * mu
    r = us[:, 0:512]
    k0 = us[:, 512:1024]
    v = us[:, 1024:1536]
    lo = us[:, 1536:1664]
    gate = us[:, 1664:2176]
    lane = lax.broadcasted_iota(jnp.int32, (1, LANES), 1)
    th = jnp.tanh(lo)
    lin = jnp.where(lane < LORA, th, lo)
    ll = jnp.dot(lin.astype(BF16), wl, preferred_element_type=F32)
    sz = _sigmoid(w0 + ll[:, :512])
    e = sz * math.exp(-0.5)
    dec = jnp.exp(-e)
    a = _sigmoid(a0 + ll[:, 512:])
    kk0 = k0 * kkw
    ss = _head_sum(kk0 * kk0, bd)
    nrm = jnp.maximum(jnp.sqrt(ss), 1e-12)
    kk = kk0 / nrm
    k = k0 * (1.0 + (a - 1.0) * kaw)
    return dict(delta=delta, us=us, r=r, k0=k0, v=v, lo=lo, gate=gate, th=th, lin=lin, sz=sz, e=e, dec=dec,
                a=a, kk0=kk0, ss=ss, nrm=nrm, kk=kk, k=k)


def _rwkv_front(x, g, w_a, mu, wl, w0, a0, kkw, kaw, tm=256):
    s, d = x.shape

    def body(x_ref, g_ref, wa_ref, mu_ref, wl_ref, w0_ref, a0_ref, kkw_ref, kaw_ref,
             h_ref, ua_ref, r_ref, w_ref, k_ref, v_ref, a_ref, b_ref, gate_ref, last_row):
        i = pl.program_id(0)
        xv = x_ref[...]
        h = (xv * lax.rsqrt(jnp.mean(xv * xv, axis=-1, keepdims=True) + RMS_EPS) * g_ref[...]).astype(BF16)
        h_ref[...] = h
        ua = jnp.dot(h, wa_ref[...], preferred_element_type=F32)
        ua_ref[...] = ua

        @pl.when(i == 0)
        def _():
            last_row[...] = jnp.zeros_like(last_row)

        f = _rwkv_elementwise(ua, last_row[...], i == 0, mu_ref[...], wl_ref[...], w0_ref[...],
                              a0_ref[...], kkw_ref[...], kaw_ref[...], _head_ones())
        last_row[...] = ua[tm - 1:tm, :]
        r_ref[...] = f["r"]
        w_ref[...] = f["dec"]
        k_ref[...] = f["k"]
        v_ref[...] = f["v"]
        a_ref[...] = -f["kk"]
        b_ref[...] = f["kk"] * f["a"]
        gate_ref[...] = f["gate"]

    vec = lambda w: pl.BlockSpec((1, w), lambda i: (0, 0))
    row = lambda w: pl.BlockSpec((tm, w), lambda i: (i, 0))
    return pl.pallas_call(
        body, name="rwkv_front", grid=(s // tm,),
        in_specs=[row(d), vec(d), pl.BlockSpec(w_a.shape, lambda i: (0, 0), pipeline_mode=pl.Buffered(1)),
                  vec(SEC), pl.BlockSpec((LANES, 2 * D_HALF), lambda i: (0, 0)),
                  vec(D_HALF), vec(D_HALF), vec(D_HALF), vec(D_HALF)],
        out_specs=[row(d), row(SEC)] + [row(D_HALF)] * 7,
        out_shape=[jax.ShapeDtypeStruct((s, d), BF16), jax.ShapeDtypeStruct((s, SEC), F32)]
                  + [jax.ShapeDtypeStruct((s, D_HALF), F32)] * 7,
        scratch_shapes=[pltpu.VMEM((1, SEC), F32)],
        compiler_params=_params("arbitrary"),
    )(x, g, w_a, mu, wl, w0, a0, kkw, kaw)


SCAN_TB = 128
N_PAIRS = 4


def _pair_sum(x, left):
    s_l = jnp.sum(jnp.where(left, x, 0.0), axis=1, keepdims=True)
    s_r = jnp.sum(jnp.where(left, 0.0, x), axis=1, keepdims=True)
    return jnp.where(left, s_l, s_r)


def _pair_dot(x, row_l, row_r, left):
    s_l = jnp.sum(x * row_l, axis=1, keepdims=True)
    s_r = jnp.sum(x * row_r, axis=1, keepdims=True)
    return jnp.where(left, s_l, s_r)


def _halves(rows8):
    lane = lax.broadcasted_iota(jnp.int32, rows8.shape, 1)
    keep_left = (lane & (LANES - 1)) < HEAD
    return jnp.where(keep_left, rows8, 0.0), jnp.where(keep_left, 0.0, rows8)


def _quad_consts():
    lane = lax.broadcasted_iota(jnp.int32, (HEAD, 2 * LANES), 1)
    rowi = lax.broadcasted_iota(jnp.int32, (HEAD, 2 * LANES), 0)
    diag2 = rowi == (lane & (HEAD - 1))
    r = lax.broadcasted_iota(jnp.int32, (2 * LANES, 2 * LANES), 0) >> 6
    c = lax.broadcasted_iota(jnp.int32, (2 * LANES, 2 * LANES), 1) >> 6
    return diag2, (r == c).astype(BF16)


def _rows_to_columns(x8, diag2, bd2):
    lhs = jnp.concatenate([jnp.where(diag2, x8[i:i + 1], 0.0).astype(BF16) for i in range(SUBLANES)], axis=0)
    return jnp.dot(lhs, bd2, preferred_element_type=F32)


def _diag_rows(qtile, diag2, bd2, sub_row2):
    res = jnp.dot(qtile, bd2, preferred_element_type=F32)
    out = jnp.zeros((SUBLANES, 2 * LANES), F32)
    for i in range(SUBLANES):
        out = jnp.where(sub_row2 == i, _colsum(jnp.where(diag2, res[i * HEAD:(i + 1) * HEAD], 0.0)), out)
    return out


def _store_tile(qbuf, slot, p, i, x):
    qbuf[slot, p // 2, i * HEAD:(i + 1) * HEAD, (p % 2) * LANES:(p % 2 + 1) * LANES] = x.astype(BF16)


def _left_half():
    return lax.broadcasted_iota(jnp.int32, (HEAD, LANES), 1) < HEAD


def _split_refs(refs, n_rows, n_out, exchange):
    n_in = len(exchange.operands) if exchange else 0
    n_ex_out = len(exchange.out_shapes) if exchange else 0
    refs = list(refs)
    rows, refs = refs[:n_rows], refs[n_rows:]
    ex_in, refs = refs[:n_in], refs[n_in:]
    outs, refs = refs[:n_out], refs[n_out:]
    ex_out, refs = refs[:n_ex_out], refs[n_ex_out:]
    scratch, sems = (refs[:-3], refs[-3:]) if exchange else (refs, None)
    moves = exchange.moves(ex_in, ex_out, sems) if exchange else None
    return rows, outs, scratch, moves


def _wkv_fwd(r, w, k, a, b, v, exchange=None):
    s = r.shape[0]
    tb = SCAN_TB
    nb = s // tb

    def body(*refs):
        (r_ref, w_ref, k_ref, a_ref, b_ref, v_ref), (y_ref, st_ref), (state, vbuf, qbuf), moves = _split_refs(
            refs, 6, 2, exchange)
        g = pl.program_id(0)
        if moves:
            moves.start(also=(g == 0))

        @pl.when(g == 0)
        def _():
            state[...] = jnp.zeros_like(state)
            qbuf[...] = jnp.zeros_like(qbuf)

        left = _left_half()
        diag2, bd2 = _quad_consts()
        sub_row2 = lax.broadcasted_iota(jnp.int32, (SUBLANES, 2 * LANES), 0)
        groups = tb // SUBLANES
        quads = [slice(g2 * 2 * LANES, (g2 + 1) * 2 * LANES) for g2 in range(2)]

        def rows_of(q):
            return pl.ds(pl.multiple_of(q * SUBLANES, SUBLANES), SUBLANES)

        def v_tiles(q, slot):
            v8 = v_ref[rows_of(q), :]
            for g2 in range(2):
                vbuf[slot, g2] = _rows_to_columns(v8[:, quads[g2]], diag2, bd2)

        def chain(q, slot):
            rows8 = rows_of(q)
            a8, w8, b8, k8, r8 = (x[rows8, :] for x in (a_ref, w_ref, b_ref, k_ref, r_ref))
            pairs = [slice(p * LANES, (p + 1) * LANES) for p in range(N_PAIRS)]
            a_next = pltpu.roll(a8, SUBLANES - 1, 0)
            (a8_l, a8_r), (wa8_l, wa8_r) = _halves(a8), _halves(w8 * a_next)
            ba8 =jnp.concatenate([_pair_sum(b8[:, pr] * a_next[:, pr], left[0:SUBLANES]) for pr in pairs], axis=1)
            ka8 = jnp.concatenate([_pair_sum(k8[:, pr] * a_next[:, pr], left[0:SUBLANES]) for pr in pairs], axis=1)
            sp = [state[p] for p in range(N_PAIRS)]
            for i in range(0, SUBLANES, 2):
                r0, r1 = slice(i, i + 1), slice(i + 1, i + 2)
                sums = [(_pair_dot(sp[p], a8_l[r0, pairs[p]], a8_r[r0, pairs[p]], left),
                         _pair_dot(sp[p], wa8_l[r0, pairs[p]], wa8_r[r0, pairs[p]], left)) for p in range(N_PAIRS)]
                sa0, sa1 = [s[0] for s in sums], [s[1] for s in sums]
                for p in range(N_PAIRS):
                    pr = pairs[p]
                    inner = slice((p % 2) * LANES, (p % 2 + 1) * LANES)
                    vt0 = vbuf[slot, p // 2, i * HEAD:(i + 1) * HEAD, inner]
                    vt1 = vbuf[slot, p // 2, (i + 1) * HEAD:(i + 2) * HEAD, inner]
                    sa_next = sa1[p] + sa0[p] * ba8[r0, pr] + vt0 * ka8[r0, pr]
                    s1 = sp[p] * w8[r0, pr] + sa0[p] * b8[r0, pr] + vt0 * k8[r0, pr]
                    st_ref[q * SUBLANES + i, p] = s1
                    _store_tile(qbuf, slot, p, i, s1 * r8[r0, pr])
                    s2 = s1 * w8[r1, pr] + sa_next * b8[r1, pr] + vt1 * k8[r1, pr]
                    st_ref[q * SUBLANES + i + 1, p] = s2
                    _store_tile(qbuf, slot, p, i + 1, s2 * r8[r1, pr])
                    sp[p] = s2
            for p in range(N_PAIRS):
                state[p] = sp[p]

        def y_rows(q, slot):
            for g2 in range(2):
                y_ref[rows_of(q), quads[g2]] = _diag_rows(qbuf[slot, g2], diag2, bd2, sub_row2)

        v_tiles(0, 0)

        def two_groups(j, carry):
            q0 = 2 * j
            v_tiles(q0 + 1, 1)
            chain(q0, 0)
            y_rows(jnp.maximum(q0 - 1, 0), 1)
            v_tiles(jnp.minimum(q0 + 2, groups - 1), 0)
            chain(q0 + 1, 1)
            y_rows(q0, 0)
            return carry

        lax.fori_loop(0, groups // 2, two_groups, 0)
        y_rows(groups - 1, 1)
        if moves:
            moves.wait(also=(g == nb - 1))

    rows = pl.BlockSpec((tb, D_HALF), lambda g: (g, 0))
    ex_in = exchange.operands if exchange else []
    ex_out = exchange.out_shapes if exchange else []
    res = pl.pallas_call(
        body, name="wkv_fwd", grid=(nb,),
        in_specs=[rows] * 6 + [ANY] * len(ex_in),
        out_specs=[rows, pl.BlockSpec((tb, N_PAIRS, HEAD, LANES), lambda g: (g, 0, 0, 0))] + [ANY] * len(ex_out),
        out_shape=[jax.ShapeDtypeStruct((s, D_HALF), F32),
                   jax.ShapeDtypeStruct((s, N_PAIRS, HEAD, LANES), F32)] + ex_out,
        scratch_shapes=[pltpu.VMEM((N_PAIRS, HEAD, LANES), F32),
                        pltpu.VMEM((2, 2, SUBLANES * HEAD, 2 * LANES), F32),
                        pltpu.VMEM((2, 2, SUBLANES * HEAD, 2 * LANES), BF16)]
                       + (exchange.scratch() if exchange else []),
        compiler_params=_params("arbitrary"),
    )(r, w, k, a, b, v, *ex_in)
    return res[0], res[1], list(res[2:])


def _wkv_bwd(r, w, k, a, b, v, dy, st, exchange=None):
    s = r.shape[0]
    tb = SCAN_TB
    nb = s // tb

    def body(*refs):
        ((r_ref, w_ref, k_ref, a_ref, b_ref, v_ref, dy_ref, st_ref, before_ref),
         (dr_ref, dw_ref, dk_ref, dv_ref, da_ref, db_ref), (dstate, vbuf, qbuf, sbuf),
         moves) = _split_refs(refs, 9, 6, exchange)
        g = pl.program_id(0)
        first_block = g == nb - 1
        if moves:
            moves.start(also=(g == 0))

        @pl.when(g == 0)
        def _():
            dstate[...] = jnp.zeros_like(dstate)
            qbuf[...] = jnp.zeros_like(qbuf)

        left = _left_half()
        diag2, bd2 = _quad_consts()
        sub_row = lax.broadcasted_iota(jnp.int32, (SUBLANES, LANES), 0)
        sub_row2 = lax.broadcasted_iota(jnp.int32, (SUBLANES, 2 * LANES), 0)
        groups = tb // SUBLANES
        quads = [slice(g2 * 2 * LANES, (g2 + 1) * 2 * LANES) for g2 in range(2)]
        row_refs = (dr_ref, dw_ref, dk_ref, da_ref, db_ref)

        def rows_of(q):
            return pl.ds(pl.multiple_of(q * SUBLANES, SUBLANES), SUBLANES)

        def state_before(q, i, p):
            if i > 0:
                return st_ref[q * SUBLANES + i - 1, p]
            return jnp.where(q == 0, jnp.where(first_block, 0.0, before_ref[0, p]),
                             st_ref[jnp.maximum(q * SUBLANES - 1, 0), p])

        def column_tiles(q, slot):
            rows8 = rows_of(q)
            for kind, ref in enumerate((v_ref, dy_ref)):
                x8 = ref[rows8, :]
                for g2 in range(2):
                    vbuf[slot, kind, g2] = _rows_to_columns(x8[:, quads[g2]], diag2, bd2)
            a8 = a_ref[rows8, :]
            for i in range(SUBLANES):
                for p in range(N_PAIRS):
                    _store_tile(sbuf, 0, p, i, state_before(q, i, p) * a8[i:i + 1, p * LANES:(p + 1) * LANES])
            for g2 in range(2):
                vbuf[slot, 2, g2] = jnp.dot(sbuf[0, g2], bd2, preferred_element_type=F32)

        def chain(q, slot):
            rows8 = rows_of(q)
            a8, w8, b8, k8, r8 = (x[rows8, :] for x in (a_ref, w_ref, b_ref, k_ref, r_ref))
            b8_l, b8_r = _halves(b8)
            dsp = [dstate[p] for p in range(N_PAIRS)]
            outs = [[jnp.zeros((SUBLANES, LANES), F32) for _ in row_refs] for _ in range(N_PAIRS)]
            after = [st_ref[q * SUBLANES + SUBLANES - 1, p] for p in range(N_PAIRS)]
            for i in reversed(range(SUBLANES)):
                row = slice(i, i + 1)
                pl_ = [slice(p * LANES, (p + 1) * LANES) for p in range(N_PAIRS)]
                tile = [(p // 2, slice(i * HEAD, (i + 1) * HEAD), slice((p % 2) * LANES, (p % 2 + 1) * LANES))
                        for p in range(N_PAIRS)]
                sp = [state_before(q, i, p) for p in range(N_PAIRS)]
                dyt = [vbuf[(slot, 1) + tile[p]] for p in range(N_PAIRS)]
                ds = [dsp[p] + dyt[p] * r8[row, pl_[p]] for p in range(N_PAIRS)]
                dsa = [_pair_dot(ds[p], b8_l[row, pl_[p]], b8_r[row, pl_[p]], left) for p in range(N_PAIRS)]
                sa = [vbuf[(slot, 2) + tile[p]] for p in range(N_PAIRS)]
                for p in range(N_PAIRS):
                    ar, wr, br, kr = (x[row, pl_[p]] for x in (a8, w8, b8, k8))
                    vt = vbuf[(slot, 0) + tile[p]]
                    dsp[p] = ds[p] * wr + dsa[p] * ar
                    new = (_colsum(after[p] * dyt[p]), _colsum(ds[p] * sp[p]), _colsum(ds[p] * vt),
                           _colsum(sp[p] * dsa[p]), _colsum(ds[p] * sa[p]))
                    outs[p] = [jnp.where(sub_row == i, n, o) for n, o in zip(new, outs[p])]
                    _store_tile(qbuf, slot, p, i, ds[p] * kr)
                after = sp
            for p in range(N_PAIRS):
                dstate[p] = dsp[p]
                for ref, o in zip(row_refs, outs[p]):
                    ref[rows8, p * LANES:(p + 1) * LANES] = o

        def dv_rows(q, slot):
            for g2 in range(2):
                dv_ref[rows_of(q), quads[g2]] = _diag_rows(qbuf[slot, g2], diag2, bd2, sub_row2)

        column_tiles(groups - 1, 0)

        def two_groups(j, carry):
            q0 = groups - 1 - 2 * j
            column_tiles(q0 - 1, 1)
            chain(q0, 0)
            dv_rows(jnp.minimum(q0 + 1, groups - 1), 1)
            column_tiles(jnp.maximum(q0 - 2, 0), 0)
            chain(q0 - 1, 1)
            dv_rows(q0, 0)
            return carry

        lax.fori_loop(0, groups // 2, two_groups, 0)
        dv_rows(0, 1)
        if moves:
            moves.wait(also=(g == nb - 1))

    rows = pl.BlockSpec((tb, D_HALF), lambda g: (nb - 1 - g, 0))
    ex_in = exchange.operands if exchange else []
    ex_out = exchange.out_shapes if exchange else []
    res = pl.pallas_call(
        body, name="wkv_bwd", grid=(nb,),
        in_specs=[rows] * 7 + [pl.BlockSpec((tb, N_PAIRS, HEAD, LANES), lambda g: (nb - 1 - g, 0, 0, 0)),
                               pl.BlockSpec((1, N_PAIRS, HEAD, LANES),
                                            lambda g: (jnp.maximum((nb - 1 - g) * tb - 1, 0), 0, 0, 0))]
                 + [ANY] * len(ex_in),
        out_specs=[rows] * 6 + [ANY] * len(ex_out),
        out_shape=[jax.ShapeDtypeStruct((s, D_HALF), F32)] * 6 + ex_out,
        scratch_shapes=[pltpu.VMEM((N_PAIRS, HEAD, LANES), F32),
                        pltpu.VMEM((2, 3, 2, SUBLANES * HEAD, 2 * LANES), F32),
                        pltpu.VMEM((2, 2, SUBLANES * HEAD, 2 * LANES), BF16),
                        pltpu.VMEM((1, 2, SUBLANES * HEAD, 2 * LANES), BF16)]
                       + (exchange.scratch() if exchange else []),
        compiler_params=_params("arbitrary"),
    )(r, w, k, a, b, v, dy, st, st, *ex_in)
    return list(res[:6]), list(res[6:])


def _rwkv_post_math(y, r, k, v, gate, lw, lb, rk, bd):
    mean = _head_sum(y, bd) * (1.0 / HEAD)
    yc = y - mean
    var = _head_sum(yc * yc, bd) * (1.0 / HEAD)
    rstd = lax.rsqrt(var + LNX_EPS)
    yn = yc * rstd
    rkk = _head_sum(r * k * rk, bd)
    sg = _sigmoid(gate)
    pre = yn * lw + lb + rkk * v
    return yn, rstd, rkk, sg, pre


def _rwkv_prep_bwd(u_a, h_t, grads, mu, wl, w0, a0, kkw, kaw, tm=256):
    s = u_a.shape[0]
    nb = s // tm
    d = h_t.shape[0]

    def body(ua_ref, prev_ref, ht_ref, drs_ref, dws_ref, dks_ref, dvs_ref, das_ref, dbs_ref, drb_ref, dkb_ref, dvb_ref,
             dgt_ref, mu_ref, wl_ref, w0_ref, a0_ref, kkw_ref, kaw_ref,
             du_ref, dwa_ref, dmu_ref, dwl_ref, dw0_ref, da0_ref, dkkw_ref, dkaw_ref, carry):
        i = pl.program_id(0)

        @pl.when(i == 0)
        def _():
            carry[...] = jnp.zeros_like(carry)
            for ref in (dwa_ref, dmu_ref, dwl_ref, dw0_ref, da0_ref, dkkw_ref, dkaw_ref):
                ref[...] = jnp.zeros_like(ref)

        bd = _head_ones()
        mu_v, wl_v, kkw_v, kaw_v = mu_ref[...], wl_ref[...], kkw_ref[...], kaw_ref[...]
        f = _rwkv_elementwise(ua_ref[...], prev_ref[7:8, :], i == nb - 1, mu_v, wl_v, w0_ref[...],
                              a0_ref[...], kkw_v, kaw_v, bd)
        a, kk, k0 = f["a"], f["kk"], f["k0"]
        dk = dks_ref[...] + dkb_ref[...]
        dbs = dbs_ref[...]
        dkk = dbs * a - das_ref[...]
        da = dbs * kk + dk * k0 * kaw_v
        dk0 = dk * (1.0 + (a - 1.0) * kaw_v)
        dkaw_ref[...] += _colsum(dk * k0 * (a - 1.0))
        inv = 1.0 / f["nrm"]
        proj = _head_sum(dkk * kk, bd)
        dkk0 = jnp.where(f["ss"] > 1e-24, (dkk - kk * proj) * inv, dkk * inv)
        dk0 = dk0 + dkk0 * kkw_v
        dkkw_ref[...] += _colsum(dkk0 * k0)
        dza = da * a * (1.0 - a)
        da0_ref[...] += _colsum(dza)
        dz = -dws_ref[...] * f["dec"] * f["e"] * (1.0 - f["sz"])
        dw0_ref[...] += _colsum(dz)
        dll = jnp.concatenate([dz, dza], axis=1).astype(BF16)
        dwl_ref[...] += _dot_tn(f["lin"].astype(BF16), dll)
        dlin = _dot_nt(dll, wl_v)
        lane = lax.broadcasted_iota(jnp.int32, (1, LANES), 1)
        th = f["th"]
        dlo = jnp.where(lane < LORA, dlin * (1.0 - th * th), dlin)
        dus = jnp.concatenate([drs_ref[...] + drb_ref[...], dk0, dvs_ref[...] + dvb_ref[...], dlo, dgt_ref[...]],
                              axis=1)
        dmu_ref[...] += _colsum(dus * f["delta"])
        g1 = dus * mu_v
        rows = lax.broadcasted_iota(jnp.int32, (tm, 1), 0)
        up = jnp.where(rows == tm - 1, carry[...], pltpu.roll(g1, tm - 1, 0))
        dua = dus - g1 + up
        du_ref[...] = dua
        dwa_ref[...] += jnp.dot(ht_ref[...], dua.astype(BF16), preferred_element_type=F32)
        carry[...] = g1[0:1, :]

    rev = lambda w: pl.BlockSpec((tm, w), lambda i: (nb - 1 - i, 0))
    vec = lambda w: pl.BlockSpec((1, w), lambda i: (0, 0))
    wl_spec = pl.BlockSpec((LANES, 2 * D_HALF), lambda i: (0, 0))
    return pl.pallas_call(
        body, name="rwkv_prep_bwd", grid=(nb,),
        in_specs=[rev(SEC), pl.BlockSpec((8, SEC), lambda i: (jnp.maximum((nb - 1 - i) * (tm // 8) - 1, 0), 0)),
                  pl.BlockSpec((d, tm), lambda i: (0, nb - 1 - i))]
                 + [rev(D_HALF)] * 10 + [vec(SEC), wl_spec] + [vec(D_HALF)] * 4,
        out_specs=[rev(SEC), pl.BlockSpec((d, SEC), lambda i: (0, 0)), vec(SEC), wl_spec] + [vec(D_HALF)] * 4,
        out_shape=[jax.ShapeDtypeStruct((s, SEC), F32), jax.ShapeDtypeStruct((d, SEC), F32),
                   jax.ShapeDtypeStruct((1, SEC), F32),
                   jax.ShapeDtypeStruct((LANES, 2 * D_HALF), F32)] + [jax.ShapeDtypeStruct((1, D_HALF), F32)] * 4,
        scratch_shapes=[pltpu.VMEM((1, SEC), F32)],
        compiler_params=_params("arbitrary"),
    )(u_a, u_a, h_t, *grads, mu, wl, w0, a0, kkw, kaw)


def _tri(tm, lower):
    r = lax.broadcasted_iota(jnp.int32, (tm, tm), 0)
    c = lax.broadcasted_iota(jnp.int32, (tm, tm), 1)
    return ((r >= c) if lower else (r <= c)).astype(BF16)


def _head_rms(x, g, bd):
    rinv = lax.rsqrt(_head_sum(x * x, bd) * (1.0 / HEAD) + RMS_EPS)
    xh = x * rinv
    return xh, rinv, xh * g


def _fox_front(h, w_b, fb, qg, kg, tm=256):
    s, d = h.shape

    def body(h_ref, wb_ref, fb_ref, qg_ref, kg_ref, ub_ref, q_ref, k_ref, v_ref, cc_ref, cr_ref, carry):
        i = pl.program_id(0)

        @pl.when(i == 0)
        def _():
            carry[...] = jnp.zeros_like(carry)

        ub_ref[...] = jnp.dot(h_ref[...], wb_ref[...], preferred_element_type=F32)
        bd = _head_ones()
        _, _, qn = _head_rms(ub_ref[:, 0:512], qg_ref[...], bd)
        _, _, kn = _head_rms(ub_ref[:, 512:1024], kg_ref[...], bd)
        q_ref[...] = (qn * ATT_SCALE).astype(BF16)
        k_ref[...] = kn.astype(BF16)
        v_ref[...] = ub_ref[:, 1024:1536].astype(BF16)
        lane = lax.broadcasted_iota(jnp.int32, (1, LANES), 1)
        logf = jnp.where(lane < N_HEADS, _log_sigmoid(ub_ref[:, 2048:2176] + fb_ref[...]), 0.0)
        cum = _exact_dot(logf, _tri(tm, True), ones_first=True) + carry[...]
        for h in range(N_HEADS):
            cc_ref[h] = jnp.broadcast_to(cum[:, h:h + 1], (tm, LANES))
        cr_ref[...] = jnp.transpose(cum)[0:N_HEADS, :]
        carry[...] = cum[tm - 1:tm, :]

    blk = pl.BlockSpec((tm, D_HALF), lambda i: (i, 0))
    return pl.pallas_call(
        body, name="fox_front", grid=(s // tm,),
        in_specs=[pl.BlockSpec((tm, d), lambda i: (i, 0)),
                  pl.BlockSpec(w_b.shape, lambda i: (0, 0), pipeline_mode=pl.Buffered(1)),
                  pl.BlockSpec((1, LANES), lambda i: (0, 0)),
                  pl.BlockSpec((1, D_HALF), lambda i: (0, 0)), pl.BlockSpec((1, D_HALF), lambda i: (0, 0))],
        out_specs=[pl.BlockSpec((tm, SEC), lambda i: (i, 0)), blk, blk, blk,
                   pl.BlockSpec((N_HEADS, tm, LANES), lambda i: (0, i, 0)), pl.BlockSpec((N_HEADS, tm), lambda i: (0, i))],
        out_shape=[jax.ShapeDtypeStruct((s, SEC), F32)] + [jax.ShapeDtypeStruct((s, D_HALF), BF16)] * 3
                  + [jax.ShapeDtypeStruct((N_HEADS, s, LANES), F32), jax.ShapeDtypeStruct((N_HEADS, s), F32)],
        scratch_shapes=[pltpu.VMEM((1, LANES), F32)],
        compiler_params=_params("arbitrary"),
    )(h, w_b, fb, qg, kg)


ATT_T = 256


def _tiles(nblk, by_query):
    if by_query:
        pairs = [(i, j) for i in range(nblk) for j in range(i + 1)]
    else:
        pairs = [(i, j) for j in range(nblk) for i in range(j, nblk)]
    return (jnp.asarray([p[0] for p in pairs], jnp.int32), jnp.asarray([p[1] for p in pairs], jnp.int32))


def _attn_fwd(q, k, v, cc, cr):
    s = q.shape[0]
    t = ATT_T
    nblk = s // t

    def body(qi_ref, kj_ref, q_ref, k_ref, v_ref, cc_ref, cr_ref, o_ref, lse_ref, m_sc, l_sc, acc_sc):
        i = qi_ref[pl.program_id(0)]
        j = kj_ref[pl.program_id(0)]

        @pl.when(j == 0)
        def _():
            m_sc[...] = jnp.full_like(m_sc, NEG)
            l_sc[...] = jnp.zeros_like(l_sc)
            acc_sc[...] = jnp.zeros_like(acc_sc)

        def tile(on_diagonal):
            causal = _causal_tile(t) if on_diagonal else None
            left = lax.broadcasted_iota(jnp.int32, (1, LANES), 1) < HEAD
            for p in range(N_PAIRS):
                lanes = slice(p * LANES, (p + 1) * LANES)
                q2, k2, v2 = q_ref[:, lanes], k_ref[:, lanes], v_ref[:, lanes]
                acc2 = acc_sc[:, lanes]
                for e in range(2):
                    h = 2 * p + e
                    msk = left if e == 0 else jnp.logical_not(left)
                    sc = _dot_nt(jnp.where(msk, q2, jnp.zeros_like(q2)), k2)
                    sc = sc + (_wide(cc_ref[h]) - cr_ref[h:h + 1, :])
                    if on_diagonal:
                        sc = jnp.where(causal, sc, NEG)
                    m_prev = m_sc[h]
                    m_new = jnp.maximum(m_prev, jnp.max(sc, axis=1, keepdims=True))
                    alpha = jnp.exp(m_prev - m_new)
                    pm = jnp.exp(sc - _wide(m_new))
                    l_sc[h] = alpha * l_sc[h] + jnp.sum(pm, axis=1, keepdims=True)
                    m_sc[h] = m_new
                    pv = jnp.dot(pm.astype(BF16), v2, preferred_element_type=F32)
                    acc2 = jnp.where(msk, alpha * acc2 + pv, acc2)
                acc_sc[:, lanes] = acc2

        pl.when(j < i)(functools.partial(tile, False))
        pl.when(j == i)(functools.partial(tile, True))

        @pl.when(j == i)
        def _():
            left = lax.broadcasted_iota(jnp.int32, (1, LANES), 1) < HEAD
            for p in range(N_PAIRS):
                lanes = slice(p * LANES, (p + 1) * LANES)
                inv = jnp.where(left, 1.0 / l_sc[2 * p], 1.0 / l_sc[2 * p + 1])
                o_ref[:, lanes] = acc_sc[:, lanes] * inv
            for h in range(N_HEADS):
                lse_ref[h] = m_sc[h] + jnp.log(l_sc[h])

    qi, kj = _tiles(nblk, by_query=True)
    qblk = pl.BlockSpec((t, D_HALF), lambda n, qi, kj: (qi[n], 0))
    kblk = pl.BlockSpec((t, D_HALF), lambda n, qi, kj: (kj[n], 0))
    qrep = pl.BlockSpec((N_HEADS, t, LANES), lambda n, qi, kj: (0, qi[n], 0))
    return pl.pallas_call(
        body, name="fox_attn_fwd",
        grid_spec=pltpu.PrefetchScalarGridSpec(
            num_scalar_prefetch=2, grid=(qi.shape[0],),
            in_specs=[qblk, kblk, kblk, qrep, pl.BlockSpec((N_HEADS, t), lambda n, qi, kj: (0, kj[n]))],
            out_specs=[qblk, qrep],
            scratch_shapes=[pltpu.VMEM((N_HEADS, t, LANES), F32), pltpu.VMEM((N_HEADS, t, LANES), F32),
                            pltpu.VMEM((t, D_HALF), F32)]),
        out_shape=[jax.ShapeDtypeStruct((s, D_HALF), F32), jax.ShapeDtypeStruct((N_HEADS, s, LANES), F32)],
        compiler_params=_params("arbitrary"),
    )(qi, kj, q, k, v, cc, cr)


def _causal_tile(t):
    return lax.broadcasted_iota(jnp.int32, (t, t), 0) >= lax.broadcasted_iota(jnp.int32, (t, t), 1)


def _wide(x):
    return jnp.concatenate([x, x], axis=1)


def _attn_probs(q2, k2, v2, do2, msk, causal, bias, lse_rows):
    zero = jnp.zeros_like(q2)
    qh = jnp.where(msk, q2, zero)
    doh = jnp.where(msk, do2, zero)
    sc = _dot_nt(qh, k2) + bias
    if causal is not None:
        sc = jnp.where(causal, sc, NEG)
    pm = jnp.exp(sc - _wide(lse_rows))
    dp = _dot_nt(doh, v2)
    return qh, doh, pm, dp


def _attn_bwd_rowdot(q, k, v, do, lse, cc, cr):
    s = q.shape[0]
    t = ATT_T
    nblk = s // t

    def body(qi_ref, kj_ref, q_ref, k_ref, v_ref, do_ref, lse_ref, cc_ref, cr_ref, dd_ref, acc):
        i = qi_ref[pl.program_id(0)]
        j = kj_ref[pl.program_id(0)]

        @pl.when(j == 0)
        def _():
            acc[...] = jnp.zeros_like(acc)

        def tile(on_diagonal):
            causal = _causal_tile(t) if on_diagonal else None
            left = lax.broadcasted_iota(jnp.int32, (1, LANES), 1) < HEAD
            for p in range(N_PAIRS):
                lanes = slice(p * LANES, (p + 1) * LANES)
                q2, k2, v2, do2 = q_ref[:, lanes], k_ref[:, lanes], v_ref[:, lanes], do_ref[:, lanes]
                for e in range(2):
                    h = 2 * p + e
                    msk = left if e == 0 else jnp.logical_not(left)
                    bias = _wide(cc_ref[h]) - cr_ref[h:h + 1, :]
                    _, _, pm, dp = _attn_probs(q2, k2, v2, do2, msk, causal, bias, lse_ref[h])
                    acc[h] += jnp.sum(pm * dp, axis=1, keepdims=True)

        pl.when(j < i)(functools.partial(tile, False))
        pl.when(j == i)(functools.partial(tile, True))

        @pl.when(j == i)
        def _():
            dd_ref[...] = acc[...]

    qi, kj = _tiles(nblk, by_query=True)
    qblk = pl.BlockSpec((t, D_HALF), lambda n, qi, kj: (qi[n], 0))
    qcol = pl.BlockSpec((N_HEADS, t, LANES), lambda n, qi, kj: (0, qi[n], 0))
    kblk = pl.BlockSpec((t, D_HALF), lambda n, qi, kj: (kj[n], 0))
    return pl.pallas_call(
        body, name="fox_attn_rowdot",
        grid_spec=pltpu.PrefetchScalarGridSpec(
            num_scalar_prefetch=2, grid=(qi.shape[0],),
            in_specs=[qblk, kblk, kblk, qblk, qcol, qcol, pl.BlockSpec((N_HEADS, t), lambda n, qi, kj: (0, kj[n]))],
            out_specs=qcol, scratch_shapes=[pltpu.VMEM((N_HEADS, t, LANES), F32)]),
        out_shape=jax.ShapeDtypeStruct((N_HEADS, s, LANES), F32),
        compiler_params=_params("arbitrary"),
    )(qi, kj, q, k, v, do, lse, cc, cr)


def _attn_bwd(q, k, v, do, lse, dd, cc, cr):
    s = q.shape[0]
    t = ATT_T
    nblk = s // t

    def body(qi_ref, kj_ref, q_ref, k_ref, v_ref, do_ref, lse_ref, dd_ref, cc_ref, cr_ref,
             dq_ref, dk_ref, dv_ref, dcr_ref, dk_sc, dv_sc, dcr_sc):
        i = qi_ref[pl.program_id(0)]
        j = kj_ref[pl.program_id(0)]

        @pl.when(pl.program_id(0) == 0)
        def _():
            dq_ref[...] = jnp.zeros_like(dq_ref)

        @pl.when(i == j)
        def _():
            dk_sc[...] = jnp.zeros_like(dk_sc)
            dv_sc[...] = jnp.zeros_like(dv_sc)
            dcr_sc[...] = jnp.zeros_like(dcr_sc)

        def tile(on_diagonal):
            causal = _causal_tile(t) if on_diagonal else None
            left = lax.broadcasted_iota(jnp.int32, (1, LANES), 1) < HEAD
            qrows = pl.ds(pl.multiple_of(i * t, t), t)
            for p in range(N_PAIRS):
                lanes = slice(p * LANES, (p + 1) * LANES)
                q2, k2, v2, do2 = q_ref[:, lanes], k_ref[:, lanes], v_ref[:, lanes], do_ref[:, lanes]
                zero = jnp.zeros_like(q2)
                dq2 = jnp.zeros((t, LANES), F32)
                dk2 = jnp.zeros((t, LANES), F32)
                dv2 = jnp.zeros((t, LANES), F32)
                for e in range(2):
                    h = 2 * p + e
                    msk = left if e == 0 else jnp.logical_not(left)
                    bias = _wide(cc_ref[h]) - cr_ref[h:h + 1, :]
                    qh, doh, pm, dp = _attn_probs(q2, k2, v2, do2, msk, causal, bias, lse_ref[h])
                    dsc = pm * (dp - _wide(dd_ref[h]))
                    dsb = dsc.astype(BF16)
                    dv2 += _dot_tn(pm.astype(BF16), doh)
                    dk2 += _dot_tn(dsb, qh)
                    dq2 += jnp.dot(dsb, jnp.where(msk, k2, zero), preferred_element_type=F32)
                    dcr_sc[h:h + 1, :] += -_colsum(dsc)
                dq_ref[qrows, lanes] += dq2 * ATT_SCALE
                dk_sc[:, lanes] += dk2
                dv_sc[:, lanes] += dv2

        pl.when(i > j)(functools.partial(tile, False))
        pl.when(i == j)(functools.partial(tile, True))

        @pl.when(i == nblk - 1)
        def _():
            dk_ref[...] = dk_sc[...]
            dv_ref[...] = dv_sc[...]
            dcr_ref[...] = dcr_sc[...]

    qi, kj = _tiles(nblk, by_query=False)
    qblk = pl.BlockSpec((t, D_HALF), lambda n, qi, kj: (qi[n], 0))
    qcol = pl.BlockSpec((N_HEADS, t, LANES), lambda n, qi, kj: (0, qi[n], 0))
    kblk = pl.BlockSpec((t, D_HALF), lambda n, qi, kj: (kj[n], 0))
    krow = pl.BlockSpec((N_HEADS, t), lambda n, qi, kj: (0, kj[n]))
    return pl.pallas_call(
        body, name="fox_attn_bwd",
        grid_spec=pltpu.PrefetchScalarGridSpec(
            num_scalar_prefetch=2, grid=(qi.shape[0],),
            in_specs=[qblk, kblk, kblk, qblk, qcol, qcol, qcol, krow],
            out_specs=[pl.BlockSpec((s, D_HALF), lambda n, qi, kj: (0, 0)), kblk, kblk, krow],
            scratch_shapes=[pltpu.VMEM((t, D_HALF), F32), pltpu.VMEM((t, D_HALF), F32), pltpu.VMEM((N_HEADS, t), F32)]),
        out_shape=[jax.ShapeDtypeStruct((s, D_HALF), F32)] * 3 + [jax.ShapeDtypeStruct((N_HEADS, s), F32)],
        compiler_params=_params("arbitrary"),
    )(qi, kj, q, k, v, do, lse, dd, cc, cr)


def _fox_prep_bwd(u_b, h_t, dq, dk, dv, dgate, dcum, fb, qg, kg, tm=256):
    s = u_b.shape[0]
    nb = s // tm
    d = h_t.shape[0]

    def body(ub_ref, ht_ref, dq_ref, dk_ref, dv_ref, dg_ref, dc_ref, fb_ref, qg_ref, kg_ref,
             du_ref, dwb_ref, dqg_ref, dkg_ref, dfb_ref, carry):
        i = pl.program_id(0)

        @pl.when(i == 0)
        def _():
            carry[...] = jnp.zeros_like(carry)
            dwb_ref[...] = jnp.zeros_like(dwb_ref)
            dqg_ref[...] = jnp.zeros_like(dqg_ref)
            dkg_ref[...] = jnp.zeros_like(dkg_ref)
            dfb_ref[...] = jnp.zeros_like(dfb_ref)

        bd = _head_ones()
        for lo, g_ref, d_ref, dgain_ref in ((0, qg_ref, dq_ref, dqg_ref), (512, kg_ref, dk_ref, dkg_ref)):
            gain = g_ref[...]
            xh, rinv, _ = _head_rms(ub_ref[:, lo:lo + 512], gain, bd)
            dn = d_ref[...]
            dgain_ref[...] += _colsum(dn * xh)
            dxh = dn * gain
            du_ref[:, lo:lo + 512] = rinv * (dxh - xh * (_head_sum(dxh * xh, bd) * (1.0 / HEAD)))
        du_ref[:, 1024:1536] = dv_ref[...]
        du_ref[:, 1536:2048] = dg_ref[...]
        lane = lax.broadcasted_iota(jnp.int32, (1, LANES), 1)
        dc = dc_ref[...]
        dlogf = _exact_dot(dc, _tri(tm, False), ones_first=True) + carry[...]
        carry[...] += _colsum(dc)
        fl = ub_ref[:, 2048:2176] + fb_ref[...]
        dfl = jnp.where(lane < N_HEADS, dlogf * (1.0 - _sigmoid(fl)), 0.0)
        du_ref[:, 2048:2176] = dfl
        dfb_ref[...] += _colsum(dfl)
        dwb_ref[...] += jnp.dot(ht_ref[...], du_ref[...].astype(BF16), preferred_element_type=F32)

    rev = lambda w: pl.BlockSpec((tm, w), lambda i: (nb - 1 - i, 0))
    vec = lambda w: pl.BlockSpec((1, w), lambda i: (0, 0))
    return pl.pallas_call(
        body, name="fox_prep_bwd", grid=(nb,),
        in_specs=[rev(SEC), pl.BlockSpec((d, tm), lambda i: (0, nb - 1 - i))] + [rev(D_HALF)] * 4
                 + [rev(LANES), vec(LANES), vec(D_HALF), vec(D_HALF)],
        out_specs=[rev(SEC), pl.BlockSpec((d, SEC), lambda i: (0, 0)), vec(D_HALF), vec(D_HALF), vec(LANES)],
        out_shape=[jax.ShapeDtypeStruct((s, SEC), F32), jax.ShapeDtypeStruct((d, SEC), F32),
                   jax.ShapeDtypeStruct((1, D_HALF), F32), jax.ShapeDtypeStruct((1, D_HALF), F32),
                   jax.ShapeDtypeStruct((1, LANES), F32)],
        scratch_shapes=[pltpu.VMEM((1, LANES), F32)],
        compiler_params=_params("arbitrary"),
    )(u_b, h_t, dq, dk, dv, dgate, dcum, fb, qg, kg)


def _merge(y, r, k, v, gate_a, o, u_b, h, x, tgt, w_g, wa, wb, wo, fg, lw, lb, rk, tm=256):
    s, d = x.shape

    def body(y_ref, r_ref, k_ref, v_ref, ga_ref, o_ref, gb_ref, h_ref, x_ref, t_ref, wg_ref, wa_ref, wb_ref, wo_ref,
             fg_ref, lw_ref, lb_ref, rk_ref,
             dx2_ref, dy_ref, drb_ref, dkb_ref, dvb_ref, dga_ref, do_ref, dgb_ref, dug_ref,
             dwa_ref, dwb_ref, dwo_ref, dfg_ref, loss_ref, dlw_ref, dlb_ref, drk_ref):
        i = pl.program_id(0)

        @pl.when(i == 0)
        def _():
            for ref in (dwa_ref, dwb_ref, dwo_ref, dfg_ref, loss_ref, dlw_ref, dlb_ref, drk_ref):
                ref[...] = jnp.zeros_like(ref)

        bd = _head_ones()
        wa_v, wb_v, wo_v, fg_v = wa_ref[...], wb_ref[...], wo_ref[...], fg_ref[...]
        rv, kv, vv, ga, lw_v, rk_v = r_ref[...], k_ref[...], v_ref[...], ga_ref[...], lw_ref[...], rk_ref[...]
        yn, rstd, rkk, sga, pre = _rwkv_post_math(y_ref[...], rv, kv, vv, ga, lw_v, lb_ref[...], rk_v, bd)
        silu_a = ga * sga
        gb, ov = gb_ref[...], o_ref[...]
        sgb = _sigmoid(gb)
        silu_b = gb * sgb
        ma = (pre * silu_a).astype(BF16)
        mb = (ov * silu_b).astype(BF16)
        ya = jnp.dot(ma, wa_v, preferred_element_type=F32)
        yb = jnp.dot(mb, wb_v, preferred_element_type=F32)
        ug = jnp.dot(h_ref[...], wg_ref[...], preferred_element_type=F32)
        sa = _sigmoid(ug[:, 0:d])
        sb = _sigmoid(ug[:, d:2 * d])
        merged = (sa * ya + sb * yb).astype(BF16)
        x2 = x_ref[...] + jnp.dot(merged, wo_v, preferred_element_type=F32)
        r2 = lax.rsqrt(jnp.mean(x2 * x2, axis=-1, keepdims=True) + RMS_EPS)
        x2h = x2 * r2
        err = x2h * fg_v - t_ref[...]
        loss_ref[...] += _colsum(err * err)
        dyo = err * (1.0 / d)
        dfg_ref[...] += _colsum(dyo * x2h)
        dx2h = dyo * fg_v
        dx2 = r2 * (dx2h - x2h * jnp.mean(dx2h * x2h, axis=-1, keepdims=True))
        dx2_ref[...] = dx2
        dx2b = dx2.astype(BF16)
        dmerged = _dot_nt(dx2b, wo_v)
        dwo_ref[...] += _dot_tn(merged, dx2b)
        dya = dmerged * sa
        dyb = dmerged * sb
        dug_ref[:, 0:d] = dya * ya * (1.0 - sa)
        dug_ref[:, d:2 * d] = dyb * yb * (1.0 - sb)
        dyab = dya.astype(BF16)
        dybb = dyb.astype(BF16)
        dwa_ref[...] += _dot_tn(ma, dyab)
        dwb_ref[...] += _dot_tn(mb, dybb)
        dmb = _dot_nt(dybb, wb_v)
        do_ref[...] = (dmb * silu_b).astype(BF16)
        dgb_ref[...] = dmb * ov * (sgb * (1.0 + gb * (1.0 - sgb)))
        dma = _dot_nt(dyab, wa_v)
        dga_ref[...] = dma * pre * (sga * (1.0 + ga * (1.0 - sga)))
        dpre = dma * silu_a
        dlw_ref[...] += _colsum(dpre * yn)
        dlb_ref[...] += _colsum(dpre)
        dyn = dpre * lw_v
        m1 = _head_sum(dyn, bd) * (1.0 / HEAD)
        m2 = _head_sum(dyn * yn, bd) * (1.0 / HEAD)
        dy_ref[...] = rstd * (dyn - m1 - yn * m2)
        dvb_ref[...] = dpre * rkk
        drkk = _head_sum(dpre * vv, bd)
        drb_ref[...] = drkk * kv * rk_v
        dkb_ref[...] = drkk * rv * rk_v
        drk_ref[...] += _colsum(drkk * rv * kv)

    row = lambda w: pl.BlockSpec((tm, w), lambda i: (i, 0))
    full = lambda a: pl.BlockSpec(a.shape, lambda i: (0, 0))
    once = lambda a: pl.BlockSpec(a.shape, lambda i: (0, 0), pipeline_mode=pl.Buffered(1))
    half = jax.ShapeDtypeStruct((s, D_HALF), F32)
    fshape = lambda a: jax.ShapeDtypeStruct(a.shape, F32)
    return pl.pallas_call(
        body, name="merge_fwd_bwd", grid=(s // tm,),
        in_specs=[row(D_HALF)] * 6 + [pl.BlockSpec((tm, D_HALF), lambda i: (i, 3)), row(d), row(d), row(d),
                                      once(w_g), once(wa), once(wb), once(wo), full(fg), full(lw), full(lb), full(rk)],
        out_specs=[row(d)] + [row(D_HALF)] * 7 + [row(GATE_COLS), full(wa), full(wb), full(wo), full(fg), full(fg),
                                                   full(lw), full(lb), full(rk)],
        out_shape=[jax.ShapeDtypeStruct((s, d), F32)] + [half] * 5 + [jax.ShapeDtypeStruct((s, D_HALF), BF16), half,
                                                                    jax.ShapeDtypeStruct((s, GATE_COLS), F32),
                                                                    fshape(wa), fshape(wb), fshape(wo), fshape(fg),
                                                                    fshape(fg), fshape(lw), fshape(lb), fshape(rk)],
        compiler_params=_params("arbitrary"),
    )(y, r, k, v, gate_a, o, u_b, h, x, tgt, w_g, wa, wb, wo, fg, lw, lb, rk)


def _lora_weight(w_up, a_up):
    z = jnp.zeros((LORA, D_HALF), w_up.dtype)
    return jnp.concatenate([jnp.concatenate([w_up, z], axis=1), jnp.concatenate([z, a_up], axis=1)], axis=0)


def _device_grads(x, tgt, p, w_a, w_up, a_up, late_weights, fwd_exchange=None, bwd_exchange=None, tail_exchange=None):
    wl = _lora_weight(w_up, a_up)
    rk = p["r_k"].reshape(1, D_HALF)
    fb = jnp.pad(p["f_bias"], ((0, 0), (0, LANES - N_HEADS)))
    qg = jnp.tile(p["q_norm_g"], (1, N_HEADS))
    kg = jnp.tile(p["k_norm_g"], (1, N_HEADS))
    fg = p["final_norm_g"].reshape(1, D_MODEL)
    mixer = (p["shift_mu"], wl, p["w0"], p["a0"], p["k_k"], p["k_a"])

    h, u_a, r, dec, k, v, av, bv, gate_a = _rwkv_front(x, p["norm_g"], w_a, *mixer)
    y, st, arrived = _wkv_fwd(r, dec, k, av, bv, v, fwd_exchange)

    w_b, w_g, w_out_a, w_out_b, w_out = late_weights(arrived)
    u_b, q, kn, vb, cc, cr = _fox_front(h, w_b, fb, qg, kg)
    o, lse = _attn_fwd(q, kn, vb, cc, cr)

    (dx2, dy, dr_b, dk_b, dv_b, dgate_a, do, dgate_b, du_g, dwa, dwb, dwo, dfg, loss_vec, dlw, dlb, drk) = _merge(
        y, r, k, v, gate_a, o, u_b, h, x, tgt, w_g, w_out_a, w_out_b, w_out, fg, p["lnx_w"], p["lnx_b"], rk)

    dd = _attn_bwd_rowdot(q, kn, vb, do, lse, cc, cr)
    dq, dk_att, dv_att, dcr = _attn_bwd(q, kn, vb, do, lse, dd, cc, cr)
    dcum = jnp.pad(dcr.T, ((0, 0), (0, LANES - N_HEADS)))
    h_t = h.T
    du_b, dw_b, dqg, dkg, dfb = _fox_prep_bwd(u_b, h_t, dq, dk_att, dv_att, dgate_b, dcum, fb, qg, kg)
    dw_g = _matmul_tn_acc(h_t, du_g, "dw_gate")

    scan_grads, sent = _wkv_bwd(r, dec, k, av, bv, v, dy, st,
                                bwd_exchange(dw_b, dw_g, dwa, dwb, dwo) if bwd_exchange else None)
    du_a, dw_a, dmu, dwl, dw0, da0, dkkw, dkaw = _rwkv_prep_bwd(
        u_a, h_t, (*scan_grads, dr_b, dk_b, dv_b, dgate_a), *mixer)
    dw_up, da_up = dwl[:LORA, :D_HALF], dwl[LORA:, D_HALF:]
    sent_last = _run_on_sequencer(tail_exchange(dw_a, dw_up, da_up), "scatter_tail", 1) if tail_exchange else []
    grad_x, dnorm_g, _ = _inproj_bwd(du_a, du_b, du_g, w_a, w_b, w_g, x, dx2, p["norm_g"])

    grads = dict(
        norm_g=dnorm_g, w_in=(dw_a, dw_b, dw_g), shift_mu=dmu,
        w_lora_up=dw_up, w0=dw0, a_lora_up=da_up, a0=da0, k_k=dkkw, k_a=dkaw,
        r_k=drk.reshape(1, N_HEADS, HEAD), lnx_w=dlw, lnx_b=dlb, f_bias=dfb[:, :N_HEADS],
        q_norm_g=dqg.reshape(N_HEADS, HEAD).sum(axis=0, keepdims=True),
        k_norm_g=dkg.reshape(N_HEADS, HEAD).sum(axis=0, keepdims=True),
        w_out_a=dwa, w_out_b=dwb, w_out=dwo, final_norm_g=dfg.reshape(D_MODEL))
    return loss_vec, grad_x, grads, sent, sent_last


CHIP_FLIPS = ((1, 0), (0, 1), (1, 1))
ANY = pl.BlockSpec(memory_space=pl.ANY)


def _position():
    return lax.axis_index("x"), lax.axis_index("y"), lax.axis_index("c")


def _flip(v, f):
    return 1 - v if f else v


def _both(a, b):
    if a is None:
        return b
    return a if b is None else jnp.logical_and(a, b)


def _when(cond, fn):
    if cond is None:
        fn()
    else:
        pl.when(cond)(fn)


class _Moves:
    def __init__(self, send_sems, recv_sems, local_sems):
        self.send_sems, self.recv_sems, self.local_sems = send_sems, recv_sems, local_sems
        self.remote, self.local = [], []

    def send(self, src, dst, peer, landing, send_if=None, recv_if=None, first=False):
        k = len(self.remote)
        sems = dict(send_sem=self.send_sems.at[k], recv_sem=self.recv_sems.at[k], device_id=peer, device_id_type=MESH)
        out = pltpu.make_async_remote_copy(src_ref=src, dst_ref=dst, **sems)
        arrival = pltpu.make_async_remote_copy(src_ref=src, dst_ref=landing, **sems)
        self.remote.append((out, arrival, send_if, recv_if, first))

    def copy(self, src, dst, cond=None):
        cp = pltpu.make_async_copy(src, dst, self.local_sems.at[len(self.local)])
        self.local.append((cp, cond))

    def start(self, also=None):
        for cp, cond in self.local:
            _when(_both(also, cond), cp.start)
        for out, _, send_if, _, _ in self.remote:
            _when(_both(also, send_if), out.start)

    def wait_arrivals(self, also=None, first=None):
        for _, arrival, _, recv_if, is_first in self.remote:
            if first is None or first == is_first:
                _when(_both(also, recv_if), arrival.wait_recv)

    def wait_sent(self, also=None):
        for out, _, send_if, _, _ in self.remote:
            _when(_both(also, send_if), out.wait_send)
        for cp, cond in self.local:
            _when(_both(also, cond), cp.wait)

    def wait(self, also=None):
        self.wait_arrivals(also)
        self.wait_sent(also)


class _Exchange:
    def __init__(self, operands, out_shapes, n_remote, n_local, build, relays=None, in_place=(), n_staging=0):
        self.operands, self.out_shapes = list(operands), list(out_shapes)
        self.n_remote, self.n_local, self.build = n_remote, n_local, build
        self.relays, self.in_place = relays, in_place
        self.n_staging = n_staging

    def scratch(self):
        return [pltpu.SemaphoreType.DMA((self.n_remote,)), pltpu.SemaphoreType.DMA((self.n_remote,)),
                pltpu.SemaphoreType.DMA((max(self.n_local, 1),))]

    def moves(self, in_refs, out_refs, sems):
        mv = _Moves(*sems)
        self.build(mv, in_refs, out_refs)
        return mv


def _run_on_sequencer(exchange, name, collective_id):
    ins = [jax.new_ref(a, memory_space=pltpu.MemorySpace.HBM) for a in exchange.operands]
    outs = [ins[i] if i in exchange.in_place else jax.empty_ref(s, memory_space=pltpu.MemorySpace.HBM)
            for i, s in enumerate(exchange.out_shapes)]
    forward, to_sibling = exchange.relays or (None, None)
    relay_scratch = [pltpu.SemaphoreType.DMA((stage[0],)) for stage in (forward, to_sibling) if stage for _ in range(2)]

    def launch(*sems):
        x, y, c = _position()
        peers = [(_flip(x, fx), _flip(y, fy), c) for fx, fy in CHIP_FLIPS] + ([(x, y, 1 - c)] if to_sibling else [])
        barrier = pltpu.get_barrier_semaphore()
        for peer in peers:
            pl.semaphore_signal(barrier, inc=1, device_id=peer, device_id_type=MESH)
        pl.semaphore_wait(barrier, len(peers))
        moves = exchange.moves(ins, outs, sems[:3])
        moves.start()
        later = []
        if forward:
            onward = _Moves(sems[3], sems[4], None)
            forward[1](onward, ins, outs)
            moves.wait_arrivals(first=True)
            onward.start()
            moves.wait_arrivals(first=False)
            onward.wait_arrivals()
            later.append(onward)
        else:
            moves.wait_arrivals()
        if to_sibling:
            passed = _Moves(*sems[-2:], None)
            to_sibling[1](passed, ins, outs)
            passed.start()
            passed.wait_arrivals()
            later.append(passed)
        for mv in later + [moves]:
            mv.wait_sent()

    pl.kernel(launch, mesh=plsc.ScalarSubcoreMesh(axis_name="sequencer", num_cores=1), name=name,
              scratch_types=tuple(exchange.scratch() + relay_scratch),
              compiler_params=pltpu.CompilerParams(collective_id=collective_id))()
    return [o[...] for o in outs[:len(outs) - exchange.n_staging]]


def _row_major_copy(a, name):
    r, c = a.shape
    tr = _row_tile(r)

    def body(a_ref, o_ref):
        o_ref[...] = a_ref[...]

    blk = pl.BlockSpec((tr, c), lambda i: (i, 0))
    return pl.pallas_call(body, name=name, grid=(r // tr,), in_specs=[blk], out_specs=blk,
                          out_shape=jax.ShapeDtypeStruct(a.shape, a.dtype), compiler_params=_params("parallel"))(a)


def _is_chip(x, y, chip):
    return jnp.logical_and(x == chip // 2, y == chip % 2)


def _gather_exchange(from_chip, from_all, split=()):
    n1, n2 = len(from_chip), len(from_all)
    near = CHIP_FLIPS[:2]

    def quarters(t, c, first, count=1):
        n = from_chip[t][1].shape[0] // 4
        return pl.ds((2 * c + first) * n, count * n)

    def build(mv, ins, outs):
        x, y, c = _position()
        me = 2 * x + y
        for t, (chip, _) in enumerate(from_chip):
            if t not in split:
                mv.copy(ins[t], outs[t], cond=_is_chip(x, y, chip))
        for t in range(n2):
            mv.copy(ins[n1 + t], outs[n1 + t].at[me])
        for t in split:
            for first in (True, False):
                for f, (fx, fy) in enumerate(near):
                    px, py = _flip(x, fx), _flip(y, fy)
                    part = quarters(t, c, f if first else 1 - f)
                    mv.send(ins[t].at[part], outs[t].at[part], (px, py, c), landing=outs[t].at[part], first=first,
                            send_if=_is_chip(x, y, from_chip[t][0]), recv_if=_is_chip(px, py, from_chip[t][0]))
        for fx, fy in CHIP_FLIPS:
            px, py = _flip(x, fx), _flip(y, fy)
            peer = (px, py, c)
            for t, (chip, _) in enumerate(from_chip):
                if t not in split:
                    mv.send(ins[t], outs[t], peer, landing=outs[t],
                            send_if=_is_chip(x, y, chip), recv_if=_is_chip(px, py, chip))
            for t in range(n2):
                mv.send(ins[n1 + t], outs[n1 + t].at[me], peer, landing=outs[n1 + t].at[2 * px + py])

    def forward(mv, ins, outs):
        x, y, c = _position()
        for t in split:
            chip = from_chip[t][0]
            for f, (fx, fy) in enumerate(near):
                gx, gy = near[1 - f]
                part = quarters(t, c, f)
                mv.send(outs[t].at[part], outs[t].at[part], (_flip(x, gx), _flip(y, gy), c), landing=outs[t].at[part],
                        send_if=_is_chip(_flip(x, fx), _flip(y, fy), chip), recv_if=_is_chip(1 - x, 1 - y, chip))

    def to_sibling(mv, ins, outs):
        x, y, c = _position()
        for t in split:
            came = jnp.logical_not(_is_chip(x, y, from_chip[t][0]))
            mv.send(outs[t].at[quarters(t, c, 0, 2)], outs[t].at[quarters(t, c, 0, 2)], (x, y, 1 - c),
                    landing=outs[t].at[quarters(t, 1 - c, 0, 2)], send_if=came, recv_if=came)

    arrays = [a for _, a in from_chip] + list(from_all)
    shapes = [jax.ShapeDtypeStruct(a.shape, a.dtype) for _, a in from_chip]
    shapes += [jax.ShapeDtypeStruct((N_CHIPS,) + a.shape, a.dtype) for a in from_all]
    n_remote = len(CHIP_FLIPS) * (n1 - len(split) + n2) + 2 * len(near) * len(split)
    relays = ((len(near) * len(split), forward), (len(split), to_sibling)) if split else None
    return _Exchange(arrays, shapes, n_remote, n1 + n2, build, relays, in_place=split)


def _scatter_exchange(to_chip, to_all, via_neighbours=False):
    n1, n2 = len(to_chip), len(to_all)
    near = CHIP_FLIPS[:2]
    direct = near if via_neighbours else CHIP_FLIPS

    def half(t, g):
        n = to_chip[t][1].shape[0] // 2
        return pl.ds(g * n, n)

    def build(mv, ins, outs):
        x, y, c = _position()
        if via_neighbours:
            for t, (chip, _) in enumerate(to_chip):
                for g, (gx, gy) in enumerate(near):
                    ox, oy = near[1 - g]
                    mv.send(ins[t].at[half(t, g)], outs[n1 + n2 + t], (_flip(x, gx), _flip(y, gy), c),
                            landing=outs[n1 + n2 + t], first=True, send_if=_is_chip(1 - x, 1 - y, chip),
                            recv_if=_is_chip(_flip(x, ox), _flip(y, oy), chip))
        for f, (fx, fy) in enumerate(CHIP_FLIPS):
            px, py = _flip(x, fx), _flip(y, fy)
            peer = (px, py, c)
            if (fx, fy) in direct:
                for t, (chip, _) in enumerate(to_chip):
                    mv.send(ins[t], outs[t].at[f], peer, landing=outs[t].at[f],
                            send_if=_is_chip(px, py, chip), recv_if=_is_chip(x, y, chip))
            for t in range(n2):
                mv.send(ins[n1 + t].at[2 * px + py], outs[n1 + t].at[f], peer, landing=outs[n1 + t].at[f])

    def forward(mv, ins, outs):
        x, y, c = _position()
        for t, (chip, _) in enumerate(to_chip):
            for g in range(len(near)):
                ox, oy = near[1 - g]
                far_slot = outs[t].at[len(near)].at[half(t, g)]
                mv.send(outs[n1 + n2 + t], far_slot, (_flip(x, ox), _flip(y, oy), c), landing=far_slot,
                        send_if=_is_chip(_flip(x, ox), _flip(y, oy), chip), recv_if=_is_chip(x, y, chip))

    arrays = [a for _, a in to_chip] + list(to_all)
    shapes = [jax.ShapeDtypeStruct((len(CHIP_FLIPS),) + a.shape, a.dtype) for _, a in to_chip]
    shapes += [jax.ShapeDtypeStruct((len(CHIP_FLIPS),) + a.shape[1:], a.dtype) for a in to_all]
    if not via_neighbours:
        return _Exchange(arrays, shapes, len(CHIP_FLIPS) * (n1 + n2), 0, build)
    shapes += [jax.ShapeDtypeStruct((a.shape[0] // 2, a.shape[1]), a.dtype) for _, a in to_chip]
    return _Exchange(arrays, shapes, 2 * len(near) * n1 + len(CHIP_FLIPS) * n2, 0, build,
                     relays=((len(near) * n1, forward), None), n_staging=n1)


def _swap_sibling(tensors, name):
    n = len(tensors)

    def body(*refs):
        ins, outs = refs[:n], refs[n:2 * n]
        send_sems, recv_sems = refs[2 * n:]
        x, y, c = _position()
        copies = [pltpu.make_async_remote_copy(
            src_ref=ins[t], dst_ref=outs[t], send_sem=send_sems.at[t], recv_sem=recv_sems.at[t],
            device_id=(x, y, 1 - c), device_id_type=MESH) for t in range(n)]
        for cp in copies:
            cp.start()
        for cp in copies:
            cp.wait_recv()
        for cp in copies:
            cp.wait_send()

    return pl.pallas_call(
        body, name=name, in_specs=[ANY] * n, out_specs=[ANY] * n,
        out_shape=[jax.ShapeDtypeStruct(a.shape, a.dtype) for a in tensors],
        scratch_shapes=[pltpu.SemaphoreType.DMA((n,)), pltpu.SemaphoreType.DMA((n,))],
        compiler_params=pltpu.CompilerParams(has_side_effects=True),
    )(*tensors)


def _allreduce_small(slab):
    stages = 3

    def body(x_ref, o_ref, buf, send_sems, recv_sems):
        x, y, c = _position()
        peers = ((1 - x, y, c), (x, 1 - y, c), (x, y, 1 - c))
        o_ref[...] = x_ref[...]
        for k, peer in enumerate(peers):
            cp = pltpu.make_async_remote_copy(src_ref=o_ref, dst_ref=buf.at[k], send_sem=send_sems.at[k],
                                              recv_sem=recv_sems.at[k], device_id=peer, device_id_type=MESH)
            cp.start()
            cp.wait()
            o_ref[...] = o_ref[...] + buf[k]

    return pl.pallas_call(
        body, name="allreduce_small",
        in_specs=[pl.BlockSpec(memory_space=pltpu.VMEM)], out_specs=pl.BlockSpec(memory_space=pltpu.VMEM),
        out_shape=jax.ShapeDtypeStruct(slab.shape, slab.dtype),
        scratch_shapes=[pltpu.VMEM((stages,) + slab.shape, slab.dtype),
                        pltpu.SemaphoreType.DMA((stages,)), pltpu.SemaphoreType.DMA((stages,))],
        compiler_params=pltpu.CompilerParams(has_side_effects=True),
    )(slab)


def _row_tile(r):
    return min(r, 256)


def _sum4(stack, recv, me):
    _, r, c = stack.shape
    tr = _row_tile(r)

    def body(me_ref, own_ref, recv_ref, o_ref):
        o_ref[...] = (((own_ref[...] + recv_ref[0].astype(F32)) + recv_ref[1].astype(F32))
                      + recv_ref[2].astype(F32))

    return pl.pallas_call(
        body, name="sum_partials",
        grid_spec=pltpu.PrefetchScalarGridSpec(
            num_scalar_prefetch=1, grid=(r // tr,),
            in_specs=[pl.BlockSpec((None, tr, c), lambda i, me_ref: (me_ref[0], i, 0)),
                      pl.BlockSpec((len(CHIP_FLIPS), tr, c), lambda i, me_ref: (0, i, 0))],
            out_specs=pl.BlockSpec((tr, c), lambda i, me_ref: (i, 0))),
        out_shape=jax.ShapeDtypeStruct((r, c), F32), compiler_params=_params("parallel"),
    )(me, stack, recv)


def _sum_block(own, recv):
    r, c = own.shape
    tr = _row_tile(r)

    def body(own_ref, recv_ref, o_ref):
        o_ref[...] = (((own_ref[...] + recv_ref[0].astype(F32)) + recv_ref[1].astype(F32))
                      + recv_ref[2].astype(F32))

    return pl.pallas_call(
        body, name="sum_block", grid=(r // tr,),
        in_specs=[pl.BlockSpec((tr, c), lambda i: (i, 0)), pl.BlockSpec((len(CHIP_FLIPS), tr, c), lambda i: (0, i, 0))],
        out_specs=pl.BlockSpec((tr, c), lambda i: (i, 0)),
        out_shape=jax.ShapeDtypeStruct((r, c), F32), compiler_params=_params("parallel"),
    )(own, recv)


def _adamw_math(w, g, m, v):
    m = ADAM_B1 * m + (1.0 - ADAM_B1) * g
    v = ADAM_B2 * v + (1.0 - ADAM_B2) * (g * g)
    m_hat = m / (1.0 - ADAM_B1 ** ADAM_STEP)
    v_hat = v / (1.0 - ADAM_B2 ** ADAM_STEP)
    delta = -ADAM_LR * (m_hat / (jnp.sqrt(v_hat) + ADAM_EPS) + ADAM_WD * w)
    return delta, m, v


def _adamw(w, m, v, g_parts, name):
    r, c = w.shape
    tr = _row_tile(r)
    n = len(g_parts)

    def body(*refs):
        w_ref, m_ref, v_ref = refs[:3]
        g_refs = refs[3:3 + n]
        g_out, d_out, m_out, v_out = refs[3 + n:]
        g = g_refs[0][...]
        for ref in g_refs[1:]:
            g = g + ref[...]
        g_out[...] = g
        d_out[...], m_out[...], v_out[...] = _adamw_math(w_ref[...], g, m_ref[...], v_ref[...])

    blk = pl.BlockSpec((tr, c), lambda i: (i, 0))
    return pl.pallas_call(
        body, name=name, grid=(r // tr,), in_specs=[blk] * (3 + n), out_specs=[blk] * 4,
        out_shape=[jax.ShapeDtypeStruct((r, c), F32)] * 4, compiler_params=_params("parallel"),
    )(w, m, v, *g_parts)


def _adamw_small(total, w, m, v):
    sizes = [w[n].size for n in SMALL]
    flat = lambda d: [d[n].reshape(1, -1) for n in SMALL]
    k = len(SMALL)

    def body(*refs):
        total_ref, w_refs, m_refs, v_refs = refs[0], refs[1:1 + k], refs[1 + k:1 + 2 * k], refs[1 + 2 * k:1 + 3 * k]
        outs = refs[1 + 3 * k:]
        for i, size in enumerate(sizes):
            g = total_ref[i:i + 1, 0:size]
            outs[i][...] = g
            outs[k + i][...], outs[2 * k + i][...], outs[3 * k + i][...] = _adamw_math(
                w_refs[i][...], g, m_refs[i][...], v_refs[i][...])

    res = pl.pallas_call(
        body, name="adamw_small", out_shape=[jax.ShapeDtypeStruct((1, size), F32) for size in sizes] * 4,
        compiler_params=_params(),
    )(total, *flat(w), *flat(m), *flat(v))
    return [{n: res[j * k + i].reshape(w[n].shape) for i, n in enumerate(SMALL)} for j in range(4)]


SHARDED = ("w_in", "w_lora_up", "a_lora_up", "w_out_a", "w_out_b", "w_out")
ROW_SHARDED = ("w_out",)
SMALL = ("norm_g", "shift_mu", "w0", "a0", "k_k", "k_a", "r_k", "lnx_w", "lnx_b", "f_bias", "q_norm_g", "k_norm_g",
         "final_norm_g")
WEIGHTS = ("norm_g", "w_in", "shift_mu", "w_lora_up", "w0", "a_lora_up", "a0", "k_k", "k_a", "r_k", "lnx_w", "lnx_b",
           "f_bias", "q_norm_g", "k_norm_g", "w_out_a", "w_out_b", "w_out", "final_norm_g")
SLAB_ROWS = 16
SLAB_COLS = SEC


def _to_slab(named, extra=None):
    rows = [jnp.pad(named[n].reshape(1, -1), ((0, 0), (0, SLAB_COLS - named[n].size))) for n in SMALL]
    if extra is not None:
        rows.append(jnp.pad(extra.reshape(1, -1), ((0, 0), (0, SLAB_COLS - extra.size))))
    rows.append(jnp.zeros((SLAB_ROWS - len(rows), SLAB_COLS), F32))
    return jnp.concatenate(rows, axis=0)


def _by_chip(g, name):
    if name in ROW_SHARDED:
        return g.reshape(N_CHIPS, g.shape[0] // N_CHIPS, g.shape[1])
    r, c = g.shape
    return g.reshape(r, N_CHIPS, c // N_CHIPS).transpose(1, 0, 2)


def _from_chips(stack, name):
    if name in ROW_SHARDED:
        return stack.reshape(-1, stack.shape[2])
    _, r, c = stack.shape
    return stack.transpose(1, 0, 2).reshape(r, N_CHIPS * c)


def kernel(x, norm_g, w_in, shift_mu, w_lora_up, w0, a_lora_up, a0, k_k, k_a, r_k, lnx_w, lnx_b, f_bias, q_norm_g, k_norm_g, w_out_a, w_out_b, w_out, final_norm_g, loss_target, m_norm_g, m_w_in, m_shift_mu, m_w_lora_up, m_w0, m_a_lora_up, m_a0, m_k_k, m_k_a, m_r_k, m_lnx_w, m_lnx_b, m_f_bias, m_q_norm_g, m_k_norm_g, m_w_out_a, m_w_out_b, m_w_out, m_final_norm_g, v_norm_g, v_w_in, v_shift_mu, v_w_lora_up, v_w0, v_a_lora_up, v_a0, v_k_k, v_k_a, v_r_k, v_lnx_w, v_lnx_b, v_f_bias, v_q_norm_g, v_k_norm_g, v_w_out_a, v_w_out_b, v_w_out, v_final_norm_g):
    w = dict(norm_g=norm_g, w_in=w_in, shift_mu=shift_mu, w_lora_up=w_lora_up, w0=w0, a_lora_up=a_lora_up, a0=a0,
             k_k=k_k, k_a=k_a, r_k=r_k, lnx_w=lnx_w, lnx_b=lnx_b, f_bias=f_bias, q_norm_g=q_norm_g,
             k_norm_g=k_norm_g, w_out_a=w_out_a, w_out_b=w_out_b, w_out=w_out, final_norm_g=final_norm_g)
    m = dict(norm_g=m_norm_g, w_in=m_w_in, shift_mu=m_shift_mu, w_lora_up=m_w_lora_up, w0=m_w0,
             a_lora_up=m_a_lora_up, a0=m_a0, k_k=m_k_k, k_a=m_k_a, r_k=m_r_k, lnx_w=m_lnx_w, lnx_b=m_lnx_b,
             f_bias=m_f_bias, q_norm_g=m_q_norm_g, k_norm_g=m_k_norm_g, w_out_a=m_w_out_a, w_out_b=m_w_out_b,
             w_out=m_w_out, final_norm_g=m_final_norm_g)
    v = dict(norm_g=v_norm_g, w_in=v_w_in, shift_mu=v_shift_mu, w_lora_up=v_w_lora_up, w0=v_w0,
             a_lora_up=v_a_lora_up, a0=v_a0, k_k=v_k_k, k_a=v_k_a, r_k=v_r_k, lnx_w=v_lnx_w, lnx_b=v_lnx_b,
             f_bias=v_f_bias, q_norm_g=v_q_norm_g, k_norm_g=v_k_norm_g, w_out_a=v_w_out_a, w_out_b=v_w_out_b,
             w_out=v_w_out, final_norm_g=v_final_norm_g)
    shapes = {n: w[n].shape for n in WEIGHTS}

    shard = {n: w[n][0].astype(BF16) for n in SHARDED}
    late = ("w_out_a", "w_out_b", "w_out")
    loras = ("w_lora_up", "a_lora_up")
    w_in_head, w_in_tail = shard["w_in"][:, :A_TAIL], shard["w_in"][:, A_TAIL:]
    shard0, shard1_head, up_stack, aup_stack = _run_on_sequencer(_gather_exchange(
        [(0, shard["w_in"]), (1, w_in_head)], [shard[n] for n in loras], split=(0, 1)), "gather_early", 2)
    moments = (_row_major_copy(m["w_in"][0], "m_w_in_rows"), _row_major_copy(v["w_in"][0], "v_w_in_rows"))
    shard0, moments = lax.optimization_barrier((shard0, moments))
    w_a = jnp.concatenate([shard0, shard1_head], axis=1)

    def late_weights(arrived):
        shard1_tail, shard2, shard3 = arrived[:3]
        w_b = jnp.concatenate([shard1_tail, shard2[:, :B_TAIL], jnp.zeros((D_MODEL, SEC - FOX_REAL), BF16)], axis=1)
        w_g = jnp.concatenate([shard2[:, B_TAIL:], shard3], axis=1)
        return (w_b, w_g, *[_from_chips(s, n) for n, s in zip(late, arrived[3:])])

    own = {}

    def bwd_exchange(dw_b, dw_g, dwa, dwb, dwo):
        own["tail1"] = dw_b[:, :B_HEAD]
        own["block2"] = jnp.concatenate([dw_b[:, B_HEAD:FOX_REAL], dw_g[:, :G_HEAD]], axis=1)
        own["block3"] = dw_g[:, G_HEAD:]
        own.update({n: _by_chip(g, n) for n, g in zip(late, (dwa, dwb, dwo))})
        return _scatter_exchange([(1, own["tail1"].astype(BF16)), (2, own["block2"].astype(BF16)),
                                  (3, own["block3"].astype(BF16))], [own[n].astype(BF16) for n in late])

    def tail_exchange(dw_a, dw_up, da_up):
        own["block0"], own["head1"] = dw_a[:, :SHARD_COLS], dw_a[:, SHARD_COLS:]
        own.update({n: _by_chip(g, n) for n, g in zip(loras, (dw_up, da_up))})
        return _scatter_exchange([(0, own["block0"].astype(BF16)), (1, own["head1"].astype(BF16))],
                                 [own[n].astype(BF16) for n in loras], via_neighbours=True)

    small = {n: w[n] for n in SMALL}
    loss_vec, grad_x, grads, sent, sent_last = _device_grads(
        x[0], loss_target[0], small, w_a, _from_chips(up_stack, "w_lora_up"), _from_chips(aup_stack, "a_lora_up"),
        late_weights, _gather_exchange([(1, w_in_tail), (2, shard["w_in"]), (3, shard["w_in"])], [shard[n] for n in late]),
        bwd_exchange, tail_exchange)

    total = _allreduce_small(_to_slab(grads, extra=loss_vec))
    loss = (0.5 / D_MODEL) * jnp.sum(total[len(SMALL)])
    out_g, out_d, out_m, out_v = _adamw_small(total, w, m, v)

    xpos, ypos, _ = _position()
    me = (2 * xpos + ypos).astype(jnp.int32).reshape(1)
    core_sum = {n: _sum4(own[n], r, me) for n, r in zip(late, sent[3:])}
    theirs = dict(zip(late, _swap_sibling([core_sum[n] for n in late], "swap_sibling_early")))

    def update(n):
        m_n, v_n = moments if n == "w_in" else (m[n][0], v[n][0])
        g, d, m2, v2 = _adamw(w[n][0], m_n, v_n, [core_sum[n], theirs[n]], "adamw_" + n)
        out_g[n], out_d[n], out_m[n], out_v[n] = (a.reshape(shapes[n]) for a in (g, d, m2, v2))

    for n in late:
        update(n)
    done = (out_d["norm_g"], [out_d[n] for n in late])
    sent_last, (out_d["norm_g"], new_d) = lax.optimization_barrier((sent_last, done))
    out_d.update(zip(late, new_d))
    core_sum["w_in"] = lax.switch(me[0], [
        lambda: _sum_block(own["block0"], sent_last[0]),
        lambda: jnp.concatenate([_sum_block(own["head1"], sent_last[1]), _sum_block(own["tail1"], sent[0])], axis=1),
        lambda: _sum_block(own["block2"], sent[1]),
        lambda: _sum_block(own["block3"], sent[2])])
    core_sum.update({n: _sum4(own[n], r, me) for n, r in zip(loras, sent_last[2:])})
    rest = ("w_in",) + loras
    theirs.update(zip(rest, _swap_sibling([core_sum[n] for n in rest], "swap_sibling")))
    for n in rest:
        update(n)

    return (loss, grad_x.reshape(x.shape), *[out_g[n] for n in WEIGHTS], *[out_d[n] for n in WEIGHTS],
            *[out_m[n] for n in WEIGHTS], *[out_v[n] for n in WEIGHTS])
```

```python
import functools
import math

import jax
import jax.numpy as jnp
from jax import lax
from jax.experimental import pallas as pl
from jax.experimental.pallas import tpu as pltpu
from jax.experimental.pallas import tpu_sc as plsc

F32 = jnp.float32
BF16 = jnp.bfloat16

D_MODEL = 1024
D_HALF = 512
HEAD = 64
N_HEADS = 8
LORA = 64
RWKV_COLS = 2176
FOX_REAL = 2056
SEC = 2176
GATE_COLS = 2048
IN_COLS = 6280
N_CHIPS = 4
SHARD_COLS = IN_COLS // N_CHIPS
A_TAIL = RWKV_COLS - SHARD_COLS
B_HEAD = SHARD_COLS - A_TAIL
B_TAIL = FOX_REAL - B_HEAD
G_HEAD = SHARD_COLS - B_TAIL
RMS_EPS = 1e-6
LNX_EPS = 64e-5
ATT_SCALE = HEAD ** -0.5
NEG = -1e30

ADAM_LR = 0.001
ADAM_B1 = 0.9
ADAM_B2 = 0.999
ADAM_EPS = 1e-08
ADAM_WD = 0.01
ADAM_STEP = 10

LANES = 128
SUBLANES = 8
VMEM_LIMIT = 56 * 1024 * 1024
MESH = pl.DeviceIdType.MESH


def _params(*sem):
    return pltpu.CompilerParams(dimension_semantics=sem if sem else None, vmem_limit_bytes=VMEM_LIMIT)


def _sigmoid(x):
    return 1.0 / (1.0 + jnp.exp(-x))


def _log_sigmoid(x):
    return jnp.minimum(x, 0.0) - jnp.log(1.0 + jnp.exp(-jnp.abs(x)))


def _head_ones():
    r = lax.broadcasted_iota(jnp.int32, (LANES, LANES), 0) >> 6
    c = lax.broadcasted_iota(jnp.int32, (LANES, LANES), 1) >> 6
    return (r == c).astype(BF16)


def _split3(x):
    hi = x.astype(BF16)
    r1 = x - hi.astype(F32)
    mid = r1.astype(BF16)
    lo = (r1 - mid.astype(F32)).astype(BF16)
    return hi, mid, lo


def _exact_dot(x, ones_bf16, ones_first=False):
    out = None
    for piece in _split3(x):
        if ones_first:
            t = jnp.dot(ones_bf16, piece, preferred_element_type=F32)
        else:
            t = jnp.dot(piece, ones_bf16, preferred_element_type=F32)
        out = t if out is None else out + t
    return out


def _head_sum(x, bd):
    n = x.shape[1] // LANES
    parts = [_exact_dot(x[:, i * LANES:(i + 1) * LANES], bd) for i in range(n)]
    return parts[0] if n == 1 else jnp.concatenate(parts, axis=1)


def _dot_nt(a, b):
    return lax.dot_general(a, b, (((1,), (1,)), ((), ())), preferred_element_type=F32)


def _dot_tn(a, b):
    return lax.dot_general(a, b, (((0,), (0,)), ((), ())), preferred_element_type=F32)


def _colsum(x):
    return jnp.sum(x, axis=0, keepdims=True)


def _matmul_tn_acc(at, b, name, tk=512):
    m, k = at.shape
    n = b.shape[1]

    def body(a_ref, b_ref, o_ref):
        j = pl.program_id(0)

        @pl.when(j == 0)
        def _():
            o_ref[...] = jnp.zeros_like(o_ref)

        o_ref[...] += jnp.dot(a_ref[...], b_ref[...].astype(BF16), preferred_element_type=F32)

    return pl.pallas_call(
        body, name=name, grid=(k // tk,),
        in_specs=[pl.BlockSpec((m, tk), lambda j: (0, j)), pl.BlockSpec((tk, n), lambda j: (j, 0))],
        out_specs=pl.BlockSpec((m, n), lambda j: (0, 0)),
        out_shape=jax.ShapeDtypeStruct((m, n), F32), compiler_params=_params("arbitrary"),
    )(at, b)


def _inproj_bwd(du_a, du_b, du_g, w_a, w_b, w_g, x, dx2, g, exchange=None, tm=256):
    s, d = x.shape
    nb = s // tm

    def body(*refs):
        ((da_ref, db_ref, dg_ref, wa_ref, wb_ref, wg_ref, x_ref, dx2_ref, g_ref), (gx_ref, gg_ref), _,
         moves) = _split_refs(refs, 9, 2, exchange)
        i = pl.program_id(0)
        if moves:
            moves.start(also=(i == 0))

        @pl.when(i == 0)
        def _():
            gg_ref[...] = jnp.zeros_like(gg_ref)

        dh = _dot_nt(da_ref[...].astype(BF16), wa_ref[...])
        dh += _dot_nt(db_ref[...].astype(BF16), wb_ref[...])
        dh += _dot_nt(dg_ref[...].astype(BF16), wg_ref[...])
        xv = x_ref[...]
        r = lax.rsqrt(jnp.mean(xv * xv, axis=-1, keepdims=True) + RMS_EPS)
        xh = xv * r
        gg_ref[...] += _colsum(dh * xh)
        dxh = dh * g_ref[...]
        gx_ref[...] = dx2_ref[...] + r * (dxh - xh * jnp.mean(dxh * xh, axis=-1, keepdims=True))
        if moves:
            moves.wait(also=(i == nb - 1))

    row = lambda w: pl.BlockSpec((tm, w), lambda i: (i, 0))
    full = lambda a: pl.BlockSpec(a.shape, lambda i: (0, 0))
    ex_in = exchange.operands if exchange else []
    ex_out = exchange.out_shapes if exchange else []
    res = pl.pallas_call(
        body, name="inproj_bwd", grid=(nb,),
        in_specs=[row(SEC), row(SEC), row(GATE_COLS), full(w_a), full(w_b), full(w_g), row(d), row(d), full(g)]
                 + [ANY] * len(ex_in),
        out_specs=[row(d), pl.BlockSpec((1, d), lambda i: (0, 0))] + [ANY] * len(ex_out),
        out_shape=[jax.ShapeDtypeStruct((s, d), F32), jax.ShapeDtypeStruct((1, d), F32)] + ex_out,
        scratch_shapes=exchange.scratch() if exchange else [],
        compiler_params=_params("arbitrary"),
    )(du_a, du_b, du_g, w_a, w_b, w_g, x, dx2, g, *ex_in)
    return res[0], res[1], list(res[2:])


def _rwkv_elementwise(ua, prev_row, first, mu, wl, w0, a0, kkw, kaw, bd):
    tm = ua.shape[0]
    rows = lax.broadcasted_iota(jnp.int32, (tm, 1), 0)
    prev = jnp.where(first, jnp.zeros_like(prev_row), prev_row)
    shifted = jnp.where(rows == 0, prev, pltpu.roll(ua, 1, 0))
    delta = shifted - ua
    us = ua + delta * mu
    r = us[:, 0:512]
    k0 = us[:, 512:1024]
    v = us[:, 1024:1536]
    lo = us[:, 1536:1664]
    gate = us[:, 1664:2176]
    lane = lax.broadcasted_iota(jnp.int32, (1, LANES), 1)
    th = jnp.tanh(lo)
    lin = jnp.where(lane < LORA, th, lo)
    ll = jnp.dot(lin.astype(BF16), wl, preferred_element_type=F32)
    sz = _sigmoid(w0 + ll[:, :512])
    e = sz * math.exp(-0.5)
    dec = jnp.exp(-e)
    a = _sigmoid(a0 + ll[:, 512:])
    kk0 = k0 * kkw
    ss = _head_sum(kk0 * kk0, bd)
    nrm = jnp.maximum(jnp.sqrt(ss), 1e-12)
    kk = kk0 / nrm
    k = k0 * (1.0 + (a - 1.0) * kaw)
    return dict(delta=delta, us=us, r=r, k0=k0, v=v, lo=lo, gate=gate, th=th, lin=lin, sz=sz, e=e, dec=dec,
                a=a, kk0=kk0, ss=ss, nrm=nrm, kk=kk, k=k)


def _rwkv_front(x, g, w_a, mu, wl, w0, a0, kkw, kaw, tm=256):
    s, d = x.shape

    def body(x_ref, g_ref, wa_ref, mu_ref, wl_ref, w0_ref, a0_ref, kkw_ref, kaw_ref,
             h_ref, ua_ref, r_ref, w_ref, k_ref, v_ref, a_ref, b_ref, gate_ref, last_row):
        i = pl.program_id(0)
        xv = x_ref[...]
        h = (xv * lax.rsqrt(jnp.mean(xv * xv, axis=-1, keepdims=True) + RMS_EPS) * g_ref[...]).astype(BF16)
        h_ref[...] = h
        ua = jnp.dot(h, wa_ref[...], preferred_element_type=F32)
        ua_ref[...] = ua

        @pl.when(i == 0)
        def _():
            last_row[...] = jnp.zeros_like(last_row)

        f = _rwkv_elementwise(ua, last_row[...], i == 0, mu_ref[...], wl_ref[...], w0_ref[...],
                              a0_ref[...], kkw_ref[...], kaw_ref[...], _head_ones())
        last_row[...] = ua[tm - 1:tm, :]
        r_ref[...] = f["r"]
        w_ref[...] = f["dec"]
        k_ref[...] = f["k"]
        v_ref[...] = f["v"]
        a_ref[...] = -f["kk"]
        b_ref[...] = f["kk"] * f["a"]
        gate_ref[...] = f["gate"]

    vec = lambda w: pl.BlockSpec((1, w), lambda i: (0, 0))
    row = lambda w: pl.BlockSpec((tm, w), lambda i: (i, 0))
    return pl.pallas_call(
        body, name="rwkv_front", grid=(s // tm,),
        in_specs=[row(d), vec(d), pl.BlockSpec(w_a.shape, lambda i: (0, 0), pipeline_mode=pl.Buffered(1)),
                  vec(SEC), pl.BlockSpec((LANES, 2 * D_HALF), lambda i: (0, 0)),
                  vec(D_HALF), vec(D_HALF), vec(D_HALF), vec(D_HALF)],
        out_specs=[row(d), row(SEC)] + [row(D_HALF)] * 7,
        out_shape=[jax.ShapeDtypeStruct((s, d), BF16), jax.ShapeDtypeStruct((s, SEC), F32)]
                  + [jax.ShapeDtypeStruct((s, D_HALF), F32)] * 7,
        scratch_shapes=[pltpu.VMEM((1, SEC), F32)],
        compiler_params=_params("arbitrary"),
    )(x, g, w_a, mu, wl, w0, a0, kkw, kaw)


SCAN_TB = 128
N_PAIRS = 4


def _pair_sum(x, left):
    s_l = jnp.sum(jnp.where(left, x, 0.0), axis=1, keepdims=True)
    s_r = jnp.sum(jnp.where(left, 0.0, x), axis=1, keepdims=True)
    return jnp.where(left, s_l, s_r)


def _pair_dot(x, row_l, row_r, left):
    s_l = jnp.sum(x * row_l, axis=1, keepdims=True)
    s_r = jnp.sum(x * row_r, axis=1, keepdims=True)
    return jnp.where(left, s_l, s_r)


def _halves(rows8):
    lane = lax.broadcasted_iota(jnp.int32, rows8.shape, 1)
    keep_left = (lane & (LANES - 1)) < HEAD
    return jnp.where(keep_left, rows8, 0.0), jnp.where(keep_left, 0.0, rows8)


def _quad_consts():
    lane = lax.broadcasted_iota(jnp.int32, (HEAD, 2 * LANES), 1)
    rowi = lax.broadcasted_iota(jnp.int32, (HEAD, 2 * LANES), 0)
    diag2 = rowi == (lane & (HEAD - 1))
    r = lax.broadcasted_iota(jnp.int32, (2 * LANES, 2 * LANES), 0) >> 6
    c = lax.broadcasted_iota(jnp.int32, (2 * LANES, 2 * LANES), 1) >> 6
    return diag2, (r == c).astype(BF16)


def _rows_to_columns(x8, diag2, bd2):
    lhs = jnp.concatenate([jnp.where(diag2, x8[i:i + 1], 0.0).astype(BF16) for i in range(SUBLANES)], axis=0)
    return jnp.dot(lhs, bd2, preferred_element_type=F32)


def _diag_rows(qtile, diag2, bd2, sub_row2):
    res = jnp.dot(qtile, bd2, preferred_element_type=F32)
    out = jnp.zeros((SUBLANES, 2 * LANES), F32)
    for i in range(SUBLANES):
        out = jnp.where(sub_row2 == i, _colsum(jnp.where(diag2, res[i * HEAD:(i + 1) * HEAD], 0.0)), out)
    return out


def _store_tile(qbuf, slot, p, i, x):
    qbuf[slot, p // 2, i * HEAD:(i + 1) * HEAD, (p % 2) * LANES:(p % 2 + 1) * LANES] = x.astype(BF16)


def _left_half():
    return lax.broadcasted_iota(jnp.int32, (HEAD, LANES), 1) < HEAD


def _split_refs(refs, n_rows, n_out, exchange):
    n_in = len(exchange.operands) if exchange else 0
    n_ex_out = len(exchange.out_shapes) if exchange else 0
    refs = list(refs)
    rows, refs = refs[:n_rows], refs[n_rows:]
    ex_in, refs = refs[:n_in], refs[n_in:]
    outs, refs = refs[:n_out], refs[n_out:]
    ex_out, refs = refs[:n_ex_out], refs[n_ex_out:]
    scratch, sems = (refs[:-3], refs[-3:]) if exchange else (refs, None)
    moves = exchange.moves(ex_in, ex_out, sems) if exchange else None
    return rows, outs, scratch, moves


def _wkv_fwd(r, w, k, a, b, v, exchange=None):
    s = r.shape[0]
    tb = SCAN_TB
    nb = s // tb

    def body(*refs):
        (r_ref, w_ref, k_ref, a_ref, b_ref, v_ref), (y_ref, st_ref), (state, vbuf, qbuf), moves = _split_refs(
            refs, 6, 2, exchange)
        g = pl.program_id(0)
        if moves:
            moves.start(also=(g == 0))

        @pl.when(g == 0)
        def _():
            state[...] = jnp.zeros_like(state)
            qbuf[...] = jnp.zeros_like(qbuf)

        left = _left_half()
        diag2, bd2 = _quad_consts()
        sub_row2 = lax.broadcasted_iota(jnp.int32, (SUBLANES, 2 * LANES), 0)
        groups = tb // SUBLANES
        quads = [slice(g2 * 2 * LANES, (g2 + 1) * 2 * LANES) for g2 in range(2)]

        def rows_of(q):
            return pl.ds(pl.multiple_of(q * SUBLANES, SUBLANES), SUBLANES)

        def v_tiles(q, slot):
            v8 = v_ref[rows_of(q), :]
            for g2 in range(2):
                vbuf[slot, g2] = _rows_to_columns(v8[:, quads[g2]], diag2, bd2)

        def chain(q, slot):
            rows8 = rows_of(q)
            a8, w8, b8, k8, r8 = (x[rows8, :] for x in (a_ref, w_ref, b_ref, k_ref, r_ref))
            pairs = [slice(p * LANES, (p + 1) * LANES) for p in range(N_PAIRS)]
            a_next = pltpu.roll(a8, SUBLANES - 1, 0)
            (a8_l, a8_r), (wa8_l, wa8_r) = _halves(a8), _halves(w8 * a_next)
            ba8 =jnp.concatenate([_pair_sum(b8[:, pr] * a_next[:, pr], left[0:SUBLANES]) for pr in pairs], axis=1)
            ka8 = jnp.concatenate([_pair_sum(k8[:, pr] * a_next[:, pr], left[0:SUBLANES]) for pr in pairs], axis=1)
            sp = [state[p] for p in range(N_PAIRS)]
            for i in range(0, SUBLANES, 2):
                r0, r1 = slice(i, i + 1), slice(i + 1, i + 2)
                sums = [(_pair_dot(sp[p], a8_l[r0, pairs[p]], a8_r[r0, pairs[p]], left),
                         _pair_dot(sp[p], wa8_l[r0, pairs[p]], wa8_r[r0, pairs[p]], left)) for p in range(N_PAIRS)]
                sa0, sa1 = [s[0] for s in sums], [s[1] for s in sums]
                for p in range(N_PAIRS):
                    pr = pairs[p]
                    inner = slice((p % 2) * LANES, (p % 2 + 1) * LANES)
                    vt0 = vbuf[slot, p // 2, i * HEAD:(i + 1) * HEAD, inner]
                    vt1 = vbuf[slot, p // 2, (i + 1) * HEAD:(i + 2) * HEAD, inner]
                    sa_next = sa1[p] + sa0[p] * ba8[r0, pr] + vt0 * ka8[r0, pr]
                    s1 = sp[p] * w8[r0, pr] + sa0[p] * b8[r0, pr] + vt0 * k8[r0, pr]
                    st_ref[q * SUBLANES + i, p] = s1
                    _store_tile(qbuf, slot, p, i, s1 * r8[r0, pr])
                    s2 = s1 * w8[r1, pr] + sa_next * b8[r1, pr] + vt1 * k8[r1, pr]
                    st_ref[q * SUBLANES + i + 1, p] = s2
                    _store_tile(qbuf, slot, p, i + 1, s2 * r8[r1, pr])
                    sp[p] = s2
            for p in range(N_PAIRS):
                state[p] = sp[p]

        def y_rows(q, slot):
            for g2 in range(2):
                y_ref[rows_of(q), quads[g2]] = _diag_rows(qbuf[slot, g2], diag2, bd2, sub_row2)

        v_tiles(0, 0)

        def two_groups(j, carry):
            q0 = 2 * j
            v_tiles(q0 + 1, 1)
            chain(q0, 0)
            y_rows(jnp.maximum(q0 - 1, 0), 1)
            v_tiles(jnp.minimum(q0 + 2, groups - 1), 0)
            chain(q0 + 1, 1)
            y_rows(q0, 0)
            return carry

        lax.fori_loop(0, groups // 2, two_groups, 0)
        y_rows(groups - 1, 1)
        if moves:
            moves.wait(also=(g == nb - 1))

    rows = pl.BlockSpec((tb, D_HALF), lambda g: (g, 0))
    ex_in = exchange.operands if exchange else []
    ex_out = exchange.out_shapes if exchange else []
    res = pl.pallas_call(
        body, name="wkv_fwd", grid=(nb,),
        in_specs=[rows] * 6 + [ANY] * len(ex_in),
        out_specs=[rows, pl.BlockSpec((tb, N_PAIRS, HEAD, LANES), lambda g: (g, 0, 0, 0))] + [ANY] * len(ex_out),
        out_shape=[jax.ShapeDtypeStruct((s, D_HALF), F32),
                   jax.ShapeDtypeStruct((s, N_PAIRS, HEAD, LANES), F32)] + ex_out,
        scratch_shapes=[pltpu.VMEM((N_PAIRS, HEAD, LANES), F32),
                        pltpu.VMEM((2, 2, SUBLANES * HEAD, 2 * LANES), F32),
                        pltpu.VMEM((2, 2, SUBLANES * HEAD, 2 * LANES), BF16)]
                       + (exchange.scratch() if exchange else []),
        compiler_params=_params("arbitrary"),
    )(r, w, k, a, b, v, *ex_in)
    return res[0], res[1], list(res[2:])


def _wkv_bwd(r, w, k, a, b, v, dy, st, exchange=None):
    s = r.shape[0]
    tb = SCAN_TB
    nb = s // tb

    def body(*refs):
        ((r_ref, w_ref, k_ref, a_ref, b_ref, v_ref, dy_ref, st_ref, before_ref),
         (dr_ref, dw_ref, dk_ref, dv_ref, da_ref, db_ref), (dstate, vbuf, qbuf, sbuf),
         moves) = _split_refs(refs, 9, 6, exchange)
        g = pl.program_id(0)
        first_block = g == nb - 1
        if moves:
            moves.start(also=(g == 0))

        @pl.when(g == 0)
        def _():
            dstate[...] = jnp.zeros_like(dstate)
            qbuf[...] = jnp.zeros_like(qbuf)

        left = _left_half()
        diag2, bd2 = _quad_consts()
        sub_row = lax.broadcasted_iota(jnp.int32, (SUBLANES, LANES), 0)
        sub_row2 = lax.broadcasted_iota(jnp.int32, (SUBLANES, 2 * LANES), 0)
        groups = tb // SUBLANES
        quads = [slice(g2 * 2 * LANES, (g2 + 1) * 2 * LANES) for g2 in range(2)]
        row_refs = (dr_ref, dw_ref, dk_ref, da_ref, db_ref)

        def rows_of(q):
            return pl.ds(pl.multiple_of(q * SUBLANES, SUBLANES), SUBLANES)

        def state_before(q, i, p):
            if i > 0:
                return st_ref[q * SUBLANES + i - 1, p]
            return jnp.where(q == 0, jnp.where(first_block, 0.0, before_ref[0, p]),
                             st_ref[jnp.maximum(q * SUBLANES - 1, 0), p])

        def column_tiles(q, slot):
            rows8 = rows_of(q)
            for kind, ref in enumerate((v_ref, dy_ref)):
                x8 = ref[rows8, :]
                for g2 in range(2):
                    vbuf[slot, kind, g2] = _rows_to_columns(x8[:, quads[g2]], diag2, bd2)
            a8 = a_ref[rows8, :]
            for i in range(SUBLANES):
                for p in range(N_PAIRS):
                    _store_tile(sbuf, 0, p, i, state_before(q, i, p) * a8[i:i + 1, p * LANES:(p + 1) * LANES])
            for g2 in range(2):
                vbuf[slot, 2, g2] = jnp.dot(sbuf[0, g2], bd2, preferred_element_type=F32)

        def chain(q, slot):
            rows8 = rows_of(q)
            a8, w8, b8, k8, r8 = (x[rows8, :] for x in (a_ref, w_ref, b_ref, k_ref, r_ref))
            b8_l, b8_r = _halves(b8)
            dsp = [dstate[p] for p in range(N_PAIRS)]
            outs = [[jnp.zeros((SUBLANES, LANES), F32) for _ in row_refs] for _ in range(N_PAIRS)]
            after = [st_ref[q * SUBLANES + SUBLANES - 1, p] for p in range(N_PAIRS)]
            for i in reversed(range(SUBLANES)):
                row = slice(i, i + 1)
                pl_ = [slice(p * LANES, (p + 1) * LANES) for p in range(N_PAIRS)]
                tile = [(p // 2, slice(i * HEAD, (i + 1) * HEAD), slice((p % 2) * LANES, (p % 2 + 1) * LANES))
                        for p in range(N_PAIRS)]
                sp = [state_before(q, i, p) for p in range(N_PAIRS)]
                dyt = [vbuf[(slot, 1) + tile[p]] for p in range(N_PAIRS)]
                ds = [dsp[p] + dyt[p] * r8[row, pl_[p]] for p in range(N_PAIRS)]
                dsa = [_pair_dot(ds[p], b8_l[row, pl_[p]], b8_r[row, pl_[p]], left) for p in range(N_PAIRS)]
                sa = [vbuf[(slot, 2) + tile[p]] for p in range(N_PAIRS)]
                for p in range(N_PAIRS):
                    ar, wr, br, kr = (x[row, pl_[p]] for x in (a8, w8, b8, k8))
                    vt = vbuf[(slot, 0) + tile[p]]
                    dsp[p] = ds[p] * wr + dsa[p] * ar
                    new = (_colsum(after[p] * dyt[p]), _colsum(ds[p] * sp[p]), _colsum(ds[p] * vt),
                           _colsum(sp[p] * dsa[p]), _colsum(ds[p] * sa[p]))
                    outs[p] = [jnp.where(sub_row == i, n, o) for n, o in zip(new, outs[p])]
                    _store_tile(qbuf, slot, p, i, ds[p] * kr)
                after = sp
            for p in range(N_PAIRS):
                dstate[p] = dsp[p]
                for ref, o in zip(row_refs, outs[p]):
                    ref[rows8, p * LANES:(p + 1) * LANES] = o

        def dv_rows(q, slot):
            for g2 in range(2):
                dv_ref[rows_of(q), quads[g2]] = _diag_rows(qbuf[slot, g2], diag2, bd2, sub_row2)

        column_tiles(groups - 1, 0)

        def two_groups(j, carry):
            q0 = groups - 1 - 2 * j
            column_tiles(q0 - 1, 1)
            chain(q0, 0)
            dv_rows(jnp.minimum(q0 + 1, groups - 1), 1)
            column_tiles(jnp.maximum(q0 - 2, 0), 0)
            chain(q0 - 1, 1)
            dv_rows(q0, 0)
            return carry

        lax.fori_loop(0, groups // 2, two_groups, 0)
        dv_rows(0, 1)
        if moves:
            moves.wait(also=(g == nb - 1))

    rows = pl.BlockSpec((tb, D_HALF), lambda g: (nb - 1 - g, 0))
    ex_in = exchange.operands if exchange else []
    ex_out = exchange.out_shapes if exchange else []
    res = pl.pallas_call(
        body, name="wkv_bwd", grid=(nb,),
        in_specs=[rows] * 7 + [pl.BlockSpec((tb, N_PAIRS, HEAD, LANES), lambda g: (nb - 1 - g, 0, 0, 0)),
                               pl.BlockSpec((1, N_PAIRS, HEAD, LANES),
                                            lambda g: (jnp.maximum((nb - 1 - g) * tb - 1, 0), 0, 0, 0))]
                 + [ANY] * len(ex_in),
        out_specs=[rows] * 6 + [ANY] * len(ex_out),
        out_shape=[jax.ShapeDtypeStruct((s, D_HALF), F32)] * 6 + ex_out,
        scratch_shapes=[pltpu.VMEM((N_PAIRS, HEAD, LANES), F32),
                        pltpu.VMEM((2, 3, 2, SUBLANES * HEAD, 2 * LANES), F32),
                        pltpu.VMEM((2, 2, SUBLANES * HEAD, 2 * LANES), BF16),
                        pltpu.VMEM((1, 2, SUBLANES * HEAD, 2 * LANES), BF16)]
                       + (exchange.scratch() if exchange else []),
        compiler_params=_params("arbitrary"),
    )(r, w, k, a, b, v, dy, st, st, *ex_in)
    return list(res[:6]), list(res[6:])


def _rwkv_post_math(y, r, k, v, gate, lw, lb, rk, bd):
    mean = _head_sum(y, bd) * (1.0 / HEAD)
    yc = y - mean
    var = _head_sum(yc * yc, bd) * (1.0 / HEAD)
    rstd = lax.rsqrt(var + LNX_EPS)
    yn = yc * rstd
    rkk = _head_sum(r * k * rk, bd)
    sg = _sigmoid(gate)
    pre = yn * lw + lb + rkk * v
    return yn, rstd, rkk, sg, pre


def _rwkv_prep_bwd(u_a, h_t, grads, mu, wl, w0, a0, kkw, kaw, tm=256):
    s = u_a.shape[0]
    nb = s // tm
    d = h_t.shape[0]

    def body(ua_ref, prev_ref, ht_ref, drs_ref, dws_ref, dks_ref, dvs_ref, das_ref, dbs_ref, drb_ref, dkb_ref, dvb_ref,
             dgt_ref, mu_ref, wl_ref, w0_ref, a0_ref, kkw_ref, kaw_ref,
             du_ref, dwa_ref, dmu_ref, dwl_ref, dw0_ref, da0_ref, dkkw_ref, dkaw_ref, carry):
        i = pl.program_id(0)

        @pl.when(i == 0)
        def _():
            carry[...] = jnp.zeros_like(carry)
            for ref in (dwa_ref, dmu_ref, dwl_ref, dw0_ref, da0_ref, dkkw_ref, dkaw_ref):
                ref[...] = jnp.zeros_like(ref)

        bd = _head_ones()
        mu_v, wl_v, kkw_v, kaw_v = mu_ref[...], wl_ref[...], kkw_ref[...], kaw_ref[...]
        f = _rwkv_elementwise(ua_ref[...], prev_ref[7:8, :], i == nb - 1, mu_v, wl_v, w0_ref[...],
                              a0_ref[...], kkw_v, kaw_v, bd)
        a, kk, k0 = f["a"], f["kk"], f["k0"]
        dk = dks_ref[...] + dkb_ref[...]
        dbs = dbs_ref[...]
        dkk = dbs * a - das_ref[...]
        da = dbs * kk + dk * k0 * kaw_v
        dk0 = dk * (1.0 + (a - 1.0) * kaw_v)
        dkaw_ref[...] += _colsum(dk * k0 * (a - 1.0))
        inv = 1.0 / f["nrm"]
        proj = _head_sum(dkk * kk, bd)
        dkk0 = jnp.where(f["ss"] > 1e-24, (dkk - kk * proj) * inv, dkk * inv)
        dk0 = dk0 + dkk0 * kkw_v
        dkkw_ref[...] += _colsum(dkk0 * k0)
        dza = da * a * (1.0 - a)
        da0_ref[...] += _colsum(dza)
        dz = -dws_ref[...] * f["dec"] * f["e"] * (1.0 - f["sz"])
        dw0_ref[...] += _colsum(dz)
        dll = jnp.concatenate([dz, dza], axis=1).astype(BF16)
        dwl_ref[...] += _dot_tn(f["lin"].astype(BF16), dll)
        dlin = _dot_nt(dll, wl_v)
        lane = lax.broadcasted_iota(jnp.int32, (1, LANES), 1)
        th = f["th"]
        dlo = jnp.where(lane < LORA, dlin * (1.0 - th * th), dlin)
        dus = jnp.concatenate([drs_ref[...] + drb_ref[...], dk0, dvs_ref[...] + dvb_ref[...], dlo, dgt_ref[...]],
                              axis=1)
        dmu_ref[...] += _colsum(dus * f["delta"])
        g1 = dus * mu_v
        rows = lax.broadcasted_iota(jnp.int32, (tm, 1), 0)
        up = jnp.where(rows == tm - 1, carry[...], pltpu.roll(g1, tm - 1, 0))
        dua = dus - g1 + up
        du_ref[...] = dua
        dwa_ref[...] += jnp.dot(ht_ref[...], dua.astype(BF16), preferred_element_type=F32)
        carry[...] = g1[0:1, :]

    rev = lambda w: pl.BlockSpec((tm, w), lambda i: (nb - 1 - i, 0))
    vec = lambda w: pl.BlockSpec((1, w), lambda i: (0, 0))
    wl_spec = pl.BlockSpec((LANES, 2 * D_HALF), lambda i: (0, 0))
    return pl.pallas_call(
        body, name="rwkv_prep_bwd", grid=(nb,),
        in_specs=[rev(SEC), pl.BlockSpec((8, SEC), lambda i: (jnp.maximum((nb - 1 - i) * (tm // 8) - 1, 0), 0)),
                  pl.BlockSpec((d, tm), lambda i: (0, nb - 1 - i))]
                 + [rev(D_HALF)] * 10 + [vec(SEC), wl_spec] + [vec(D_HALF)] * 4,
        out_specs=[rev(SEC), pl.BlockSpec((d, SEC), lambda i: (0, 0)), vec(SEC), wl_spec] + [vec(D_HALF)] * 4,
        out_shape=[jax.ShapeDtypeStruct((s, SEC), F32), jax.ShapeDtypeStruct((d, SEC), F32),
                   jax.ShapeDtypeStruct((1, SEC), F32),
                   jax.ShapeDtypeStruct((LANES, 2 * D_HALF), F32)] + [jax.ShapeDtypeStruct((1, D_HALF), F32)] * 4,
        scratch_shapes=[pltpu.VMEM((1, SEC), F32)],
        compiler_params=_params("arbitrary"),
    )(u_a, u_a, h_t, *grads, mu, wl, w0, a0, kkw, kaw)


def _tri(tm, lower):
    r = lax.broadcasted_iota(jnp.int32, (tm, tm), 0)
    c = lax.broadcasted_iota(jnp.int32, (tm, tm), 1)
    return ((r >= c) if lower else (r <= c)).astype(BF16)


def _head_rms(x, g, bd):
    rinv = lax.rsqrt(_head_sum(x * x, bd) * (1.0 / HEAD) + RMS_EPS)
    xh = x * rinv
    return xh, rinv, xh * g


def _fox_front(h, w_b, fb, qg, kg, tm=256):
    s, d = h.shape

    def body(h_ref, wb_ref, fb_ref, qg_ref, kg_ref, ub_ref, q_ref, k_ref, v_ref, cc_ref, cr_ref, carry):
        i = pl.program_id(0)

        @pl.when(i == 0)
        def _():
            carry[...] = jnp.zeros_like(carry)

        ub_ref[...] = jnp.dot(h_ref[...], wb_ref[...], preferred_element_type=F32)
        bd = _head_ones()
        _, _, qn = _head_rms(ub_ref[:, 0:512], qg_ref[...], bd)
        _, _, kn = _head_rms(ub_ref[:, 512:1024], kg_ref[...], bd)
        q_ref[...] = (qn * ATT_SCALE).astype(BF16)
        k_ref[...] = kn.astype(BF16)
        v_ref[...] = ub_ref[:, 1024:1536].astype(BF16)
        lane = lax.broadcasted_iota(jnp.int32, (1, LANES), 1)
        logf = jnp.where(lane < N_HEADS, _log_sigmoid(ub_ref[:, 2048:2176] + fb_ref[...]), 0.0)
        cum = _exact_dot(logf, _tri(tm, True), ones_first=True) + carry[...]
        for h in range(N_HEADS):
            cc_ref[h] = jnp.broadcast_to(cum[:, h:h + 1], (tm, LANES))
        cr_ref[...] = jnp.transpose(cum)[0:N_HEADS, :]
        carry[...] = cum[tm - 1:tm, :]

    blk = pl.BlockSpec((tm, D_HALF), lambda i: (i, 0))
    return pl.pallas_call(
        body, name="fox_front", grid=(s // tm,),
        in_specs=[pl.BlockSpec((tm, d), lambda i: (i, 0)),
                  pl.BlockSpec(w_b.shape, lambda i: (0, 0), pipeline_mode=pl.Buffered(1)),
                  pl.BlockSpec((1, LANES), lambda i: (0, 0)),
                  pl.BlockSpec((1, D_HALF), lambda i: (0, 0)), pl.BlockSpec((1, D_HALF), lambda i: (0, 0))],
        out_specs=[pl.BlockSpec((tm, SEC), lambda i: (i, 0)), blk, blk, blk,
                   pl.BlockSpec((N_HEADS, tm, LANES), lambda i: (0, i, 0)), pl.BlockSpec((N_HEADS, tm), lambda i: (0, i))],
        out_shape=[jax.ShapeDtypeStruct((s, SEC), F32)] + [jax.ShapeDtypeStruct((s, D_HALF), BF16)] * 3
                  + [jax.ShapeDtypeStruct((N_HEADS, s, LANES), F32), jax.ShapeDtypeStruct((N_HEADS, s), F32)],
        scratch_shapes=[pltpu.VMEM((1, LANES), F32)],
        compiler_params=_params("arbitrary"),
    )(h, w_b, fb, qg, kg)


ATT_T = 256


def _tiles(nblk, by_query):
    if by_query:
        pairs = [(i, j) for i in range(nblk) for j in range(i + 1)]
    else:
        pairs = [(i, j) for j in range(nblk) for i in range(j, nblk)]
    return (jnp.asarray([p[0] for p in pairs], jnp.int32), jnp.asarray([p[1] for p in pairs], jnp.int32))


def _attn_fwd(q, k, v, cc, cr):
    s = q.shape[0]
    t = ATT_T
    nblk = s // t

    def body(qi_ref, kj_ref, q_ref, k_ref, v_ref, cc_ref, cr_ref, o_ref, lse_ref, m_sc, l_sc, acc_sc):
        i = qi_ref[pl.program_id(0)]
        j = kj_ref[pl.program_id(0)]

        @pl.when(j == 0)
        def _():
            m_sc[...] = jnp.full_like(m_sc, NEG)
            l_sc[...] = jnp.zeros_like(l_sc)
            acc_sc[...] = jnp.zeros_like(acc_sc)

        def tile(on_diagonal):
            causal = _causal_tile(t) if on_diagonal else None
            left = lax.broadcasted_iota(jnp.int32, (1, LANES), 1) < HEAD
            for p in range(N_PAIRS):
                lanes = slice(p * LANES, (p + 1) * LANES)
                q2, k2, v2 = q_ref[:, lanes], k_ref[:, lanes], v_ref[:, lanes]
                acc2 = acc_sc[:, lanes]
                for e in range(2):
                    h = 2 * p + e
                    msk = left if e == 0 else jnp.logical_not(left)
                    sc = _dot_nt(jnp.where(msk, q2, jnp.zeros_like(q2)), k2)
                    sc = sc + (_wide(cc_ref[h]) - cr_ref[h:h + 1, :])
                    if on_diagonal:
                        sc = jnp.where(causal, sc, NEG)
                    m_prev = m_sc[h]
                    m_new = jnp.maximum(m_prev, jnp.max(sc, axis=1, keepdims=True))
                    alpha = jnp.exp(m_prev - m_new)
                    pm = jnp.exp(sc - _wide(m_new))
                    l_sc[h] = alpha * l_sc[h] + jnp.sum(pm, axis=1, keepdims=True)
                    m_sc[h] = m_new
                    pv = jnp.dot(pm.astype(BF16), v2, preferred_element_type=F32)
                    acc2 = jnp.where(msk, alpha * acc2 + pv, acc2)
                acc_sc[:, lanes] = acc2

        pl.when(j < i)(functools.partial(tile, False))
        pl.when(j == i)(functools.partial(tile, True))

        @pl.when(j == i)
        def _():
            left = lax.broadcasted_iota(jnp.int32, (1, LANES), 1) < HEAD
            for p in range(N_PAIRS):
                lanes = slice(p * LANES, (p + 1) * LANES)
                inv = jnp.where(left, 1.0 / l_sc[2 * p], 1.0 / l_sc[2 * p + 1])
                o_ref[:, lanes] = acc_sc[:, lanes] * inv
            for h in range(N_HEADS):
                lse_ref[h] = m_sc[h] + jnp.log(l_sc[h])

    qi, kj = _tiles(nblk, by_query=True)
    qblk = pl.BlockSpec((t, D_HALF), lambda n, qi, kj: (qi[n], 0))
    kblk = pl.BlockSpec((t, D_HALF), lambda n, qi, kj: (kj[n], 0))
    qrep = pl.BlockSpec((N_HEADS, t, LANES), lambda n, qi, kj: (0, qi[n], 0))
    return pl.pallas_call(
        body, name="fox_attn_fwd",
        grid_spec=pltpu.PrefetchScalarGridSpec(
            num_scalar_prefetch=2, grid=(qi.shape[0],),
            in_specs=[qblk, kblk, kblk, qrep, pl.BlockSpec((N_HEADS, t), lambda n, qi, kj: (0, kj[n]))],
            out_specs=[qblk, qrep],
            scratch_shapes=[pltpu.VMEM((N_HEADS, t, LANES), F32), pltpu.VMEM((N_HEADS, t, LANES), F32),
                            pltpu.VMEM((t, D_HALF), F32)]),
        out_shape=[jax.ShapeDtypeStruct((s, D_HALF), F32), jax.ShapeDtypeStruct((N_HEADS, s, LANES), F32)],
        compiler_params=_params("arbitrary"),
    )(qi, kj, q, k, v, cc, cr)


def _causal_tile(t):
    return lax.broadcasted_iota(jnp.int32, (t, t), 0) >= lax.broadcasted_iota(jnp.int32, (t, t), 1)


def _wide(x):
    return jnp.concatenate([x, x], axis=1)


def _attn_probs(q2, k2, v2, do2, msk, causal, bias, lse_rows):
    zero = jnp.zeros_like(q2)
    qh = jnp.where(msk, q2, zero)
    doh = jnp.where(msk, do2, zero)
    sc = _dot_nt(qh, k2) + bias
    if causal is not None:
        sc = jnp.where(causal, sc, NEG)
    pm = jnp.exp(sc - _wide(lse_rows))
    dp = _dot_nt(doh, v2)
    return qh, doh, pm, dp


def _attn_bwd_rowdot(q, k, v, do, lse, cc, cr):
    s = q.shape[0]
    t = ATT_T
    nblk = s // t

    def body(qi_ref, kj_ref, q_ref, k_ref, v_ref, do_ref, lse_ref, cc_ref, cr_ref, dd_ref, acc):
        i = qi_ref[pl.program_id(0)]
        j = kj_ref[pl.program_id(0)]

        @pl.when(j == 0)
        def _():
            acc[...] = jnp.zeros_like(acc)

        def tile(on_diagonal):
            causal = _causal_tile(t) if on_diagonal else None
            left = lax.broadcasted_iota(jnp.int32, (1, LANES), 1) < HEAD
            for p in range(N_PAIRS):
                lanes = slice(p * LANES, (p + 1) * LANES)
                q2, k2, v2, do2 = q_ref[:, lanes], k_ref[:, lanes], v_ref[:, lanes], do_ref[:, lanes]
                for e in range(2):
                    h = 2 * p + e
                    msk = left if e == 0 else jnp.logical_not(left)
                    bias = _wide(cc_ref[h]) - cr_ref[h:h + 1, :]
                    _, _, pm, dp = _attn_probs(q2, k2, v2, do2, msk, causal, bias, lse_ref[h])
                    acc[h] += jnp.sum(pm * dp, axis=1, keepdims=True)

        pl.when(j < i)(functools.partial(tile, False))
        pl.when(j == i)(functools.partial(tile, True))

        @pl.when(j == i)
        def _():
            dd_ref[...] = acc[...]

    qi, kj = _tiles(nblk, by_query=True)
    qblk = pl.BlockSpec((t, D_HALF), lambda n, qi, kj: (qi[n], 0))
    qcol = pl.BlockSpec((N_HEADS, t, LANES), lambda n, qi, kj: (0, qi[n], 0))
    kblk = pl.BlockSpec((t, D_HALF), lambda n, qi, kj: (kj[n], 0))
    return pl.pallas_call(
        body, name="fox_attn_rowdot",
        grid_spec=pltpu.PrefetchScalarGridSpec(
            num_scalar_prefetch=2, grid=(qi.shape[0],),
            in_specs=[qblk, kblk, kblk, qblk, qcol, qcol, pl.BlockSpec((N_HEADS, t), lambda n, qi, kj: (0, kj[n]))],
            out_specs=qcol, scratch_shapes=[pltpu.VMEM((N_HEADS, t, LANES), F32)]),
        out_shape=jax.ShapeDtypeStruct((N_HEADS, s, LANES), F32),
        compiler_params=_params("arbitrary"),
    )(qi, kj, q, k, v, do, lse, cc, cr)


def _attn_bwd(q, k, v, do, lse, dd, cc, cr):
    s = q.shape[0]
    t = ATT_T
    nblk = s // t

    def body(qi_ref, kj_ref, q_ref, k_ref, v_ref, do_ref, lse_ref, dd_ref, cc_ref, cr_ref,
             dq_ref, dk_ref, dv_ref, dcr_ref, dk_sc, dv_sc, dcr_sc):
        i = qi_ref[pl.program_id(0)]
        j = kj_ref[pl.program_id(0)]

        @pl.when(pl.program_id(0) == 0)
        def _():
            dq_ref[...] = jnp.zeros_like(dq_ref)

        @pl.when(i == j)
        def _():
            dk_sc[...] = jnp.zeros_like(dk_sc)
            dv_sc[...] = jnp.zeros_like(dv_sc)
            dcr_sc[...] = jnp.zeros_like(dcr_sc)

        def tile(on_diagonal):
            causal = _causal_tile(t) if on_diagonal else None
            left = lax.broadcasted_iota(jnp.int32, (1, LANES), 1) < HEAD
            qrows = pl.ds(pl.multiple_of(i * t, t), t)
            for p in range(N_PAIRS):
                lanes = slice(p * LANES, (p + 1) * LANES)
                q2, k2, v2, do2 = q_ref[:, lanes], k_ref[:, lanes], v_ref[:, lanes], do_ref[:, lanes]
                zero = jnp.zeros_like(q2)
                dq2 = jnp.zeros((t, LANES), F32)
                dk2 = jnp.zeros((t, LANES), F32)
                dv2 = jnp.zeros((t, LANES), F32)
                for e in range(2):
                    h = 2 * p + e
                    msk = left if e == 0 else jnp.logical_not(left)
                    bias = _wide(cc_ref[h]) - cr_ref[h:h + 1, :]
                    qh, doh, pm, dp = _attn_probs(q2, k2, v2, do2, msk, causal, bias, lse_ref[h])
                    dsc = pm * (dp - _wide(dd_ref[h]))
                    dsb = dsc.astype(BF16)
                    dv2 += _dot_tn(pm.astype(BF16), doh)
                    dk2 += _dot_tn(dsb, qh)
                    dq2 += jnp.dot(dsb, jnp.where(msk, k2, zero), preferred_element_type=F32)
                    dcr_sc[h:h + 1, :] += -_colsum(dsc)
                dq_ref[qrows, lanes] += dq2 * ATT_SCALE
                dk_sc[:, lanes] += dk2
                dv_sc[:, lanes] += dv2

        pl.when(i > j)(functools.partial(tile, False))
        pl.when(i == j)(functools.partial(tile, True))

        @pl.when(i == nblk - 1)
        def _():
            dk_ref[...] = dk_sc[...]
            dv_ref[...] = dv_sc[...]
            dcr_ref[...] = dcr_sc[...]

    qi, kj = _tiles(nblk, by_query=False)
    qblk = pl.BlockSpec((t, D_HALF), lambda n, qi, kj: (qi[n], 0))
    qcol = pl.BlockSpec((N_HEADS, t, LANES), lambda n, qi, kj: (0, qi[n], 0))
    kblk = pl.BlockSpec((t, D_HALF), lambda n, qi, kj: (kj[n], 0))
    krow = pl.BlockSpec((N_HEADS, t), lambda n, qi, kj: (0, kj[n]))
    return pl.pallas_call(
        body, name="fox_attn_bwd",
        grid_spec=pltpu.PrefetchScalarGridSpec(
            num_scalar_prefetch=2, grid=(qi.shape[0],),
            in_specs=[qblk, kblk, kblk, qblk, qcol, qcol, qcol, krow],
            out_specs=[pl.BlockSpec((s, D_HALF), lambda n, qi, kj: (0, 0)), kblk, kblk, krow],
            scratch_shapes=[pltpu.VMEM((t, D_HALF), F32), pltpu.VMEM((t, D_HALF), F32), pltpu.VMEM((N_HEADS, t), F32)]),
        out_shape=[jax.ShapeDtypeStruct((s, D_HALF), F32)] * 3 + [jax.ShapeDtypeStruct((N_HEADS, s), F32)],
        compiler_params=_params("arbitrary"),
    )(qi, kj, q, k, v, do, lse, dd, cc, cr)


def _fox_prep_bwd(u_b, h_t, dq, dk, dv, dgate, dcum, fb, qg, kg, tm=256):
    s = u_b.shape[0]
    nb = s // tm
    d = h_t.shape[0]

    def body(ub_ref, ht_ref, dq_ref, dk_ref, dv_ref, dg_ref, dc_ref, fb_ref, qg_ref, kg_ref,
             du_ref, dwb_ref, dqg_ref, dkg_ref, dfb_ref, carry):
        i = pl.program_id(0)

        @pl.when(i == 0)
        def _():
            carry[...] = jnp.zeros_like(carry)
            dwb_ref[...] = jnp.zeros_like(dwb_ref)
            dqg_ref[...] = jnp.zeros_like(dqg_ref)
            dkg_ref[...] = jnp.zeros_like(dkg_ref)
            dfb_ref[...] = jnp.zeros_like(dfb_ref)

        bd = _head_ones()
        for lo, g_ref, d_ref, dgain_ref in ((0, qg_ref, dq_ref, dqg_ref), (512, kg_ref, dk_ref, dkg_ref)):
            gain = g_ref[...]
            xh, rinv, _ = _head_rms(ub_ref[:, lo:lo + 512], gain, bd)
            dn = d_ref[...]
            dgain_ref[...] += _colsum(dn * xh)
            dxh = dn * gain
            du_ref[:, lo:lo + 512] = rinv * (dxh - xh * (_head_sum(dxh * xh, bd) * (1.0 / HEAD)))
        du_ref[:, 1024:1536] = dv_ref[...]
        du_ref[:, 1536:2048] = dg_ref[...]
        lane = lax.broadcasted_iota(jnp.int32, (1, LANES), 1)
        dc = dc_ref[...]
        dlogf = _exact_dot(dc, _tri(tm, False), ones_first=True) + carry[...]
        carry[...] += _colsum(dc)
        fl = ub_ref[:, 2048:2176] + fb_ref[...]
        dfl = jnp.where(lane < N_HEADS, dlogf * (1.0 - _sigmoid(fl)), 0.0)
        du_ref[:, 2048:2176] = dfl
        dfb_ref[...] += _colsum(dfl)
        dwb_ref[...] += jnp.dot(ht_ref[...], du_ref[...].astype(BF16), preferred_element_type=F32)

    rev = lambda w: pl.BlockSpec((tm, w), lambda i: (nb - 1 - i, 0))
    vec = lambda w: pl.BlockSpec((1, w), lambda i: (0, 0))
    return pl.pallas_call(
        body, name="fox_prep_bwd", grid=(nb,),
        in_specs=[rev(SEC), pl.BlockSpec((d, tm), lambda i: (0, nb - 1 - i))] + [rev(D_HALF)] * 4
                 + [rev(LANES), vec(LANES), vec(D_HALF), vec(D_HALF)],
        out_specs=[rev(SEC), pl.BlockSpec((d, SEC), lambda i: (0, 0)), vec(D_HALF), vec(D_HALF), vec(LANES)],
        out_shape=[jax.ShapeDtypeStruct((s, SEC), F32), jax.ShapeDtypeStruct((d, SEC), F32),
                   jax.ShapeDtypeStruct((1, D_HALF), F32), jax.ShapeDtypeStruct((1, D_HALF), F32),
                   jax.ShapeDtypeStruct((1, LANES), F32)],
        scratch_shapes=[pltpu.VMEM((1, LANES), F32)],
        compiler_params=_params("arbitrary"),
    )(u_b, h_t, dq, dk, dv, dgate, dcum, fb, qg, kg)


def _merge(y, r, k, v, gate_a, o, u_b, h, x, tgt, w_g, wa, wb, wo, fg, lw, lb, rk, tm=256):
    s, d = x.shape

    def body(y_ref, r_ref, k_ref, v_ref, ga_ref, o_ref, gb_ref, h_ref, x_ref, t_ref, wg_ref, wa_ref, wb_ref, wo_ref,
             fg_ref, lw_ref, lb_ref, rk_ref,
             dx2_ref, dy_ref, drb_ref, dkb_ref, dvb_ref, dga_ref, do_ref, dgb_ref, dug_ref,
             dwa_ref, dwb_ref, dwo_ref, dfg_ref, loss_ref, dlw_ref, dlb_ref, drk_ref):
        i = pl.program_id(0)

        @pl.when(i == 0)
        def _():
            for ref in (dwa_ref, dwb_ref, dwo_ref, dfg_ref, loss_ref, dlw_ref, dlb_ref, drk_ref):
                ref[...] = jnp.zeros_like(ref)

        bd = _head_ones()
        wa_v, wb_v, wo_v, fg_v = wa_ref[...], wb_ref[...], wo_ref[...], fg_ref[...]
        rv, kv, vv, ga, lw_v, rk_v = r_ref[...], k_ref[...], v_ref[...], ga_ref[...], lw_ref[...], rk_ref[...]
        yn, rstd, rkk, sga, pre = _rwkv_post_math(y_ref[...], rv, kv, vv, ga, lw_v, lb_ref[...], rk_v, bd)
        silu_a = ga * sga
        gb, ov = gb_ref[...], o_ref[...]
        sgb = _sigmoid(gb)
        silu_b = gb * sgb
        ma = (pre * silu_a).astype(BF16)
        mb = (ov * silu_b).astype(BF16)
        ya = jnp.dot(ma, wa_v, preferred_element_type=F32)
        yb = jnp.dot(mb, wb_v, preferred_element_type=F32)
        ug = jnp.dot(h_ref[...], wg_ref[...], preferred_element_type=F32)
        sa = _sigmoid(ug[:, 0:d])
        sb = _sigmoid(ug[:, d:2 * d])
        merged = (sa * ya + sb * yb).astype(BF16)
        x2 = x_ref[...] + jnp.dot(merged, wo_v, preferred_element_type=F32)
        r2 = lax.rsqrt(jnp.mean(x2 * x2, axis=-1, keepdims=True) + RMS_EPS)
        x2h = x2 * r2
        err = x2h * fg_v - t_ref[...]
        loss_ref[...] += _colsum(err * err)
        dyo = err * (1.0 / d)
        dfg_ref[...] += _colsum(dyo * x2h)
        dx2h = dyo * fg_v
        dx2 = r2 * (dx2h - x2h * jnp.mean(dx2h * x2h, axis=-1, keepdims=True))
        dx2_ref[...] = dx2
        dx2b = dx2.astype(BF16)
        dmerged = _dot_nt(dx2b, wo_v)
        dwo_ref[...] += _dot_tn(merged, dx2b)
        dya = dmerged * sa
        dyb = dmerged * sb
        dug_ref[:, 0:d] = dya * ya * (1.0 - sa)
        dug_ref[:, d:2 * d] = dyb * yb * (1.0 - sb)
        dyab = dya.astype(BF16)
        dybb = dyb.astype(BF16)
        dwa_ref[...] += _dot_tn(ma, dyab)
        dwb_ref[...] += _dot_tn(mb, dybb)
        dmb = _dot_nt(dybb, wb_v)
        do_ref[...] = (dmb * silu_b).astype(BF16)
        dgb_ref[...] = dmb * ov * (sgb * (1.0 + gb * (1.0 - sgb)))
        dma = _dot_nt(dyab, wa_v)
        dga_ref[...] = dma * pre * (sga * (1.0 + ga * (1.0 - sga)))
        dpre = dma * silu_a
        dlw_ref[...] += _colsum(dpre * yn)
        dlb_ref[...] += _colsum(dpre)
        dyn = dpre * lw_v
        m1 = _head_sum(dyn, bd) * (1.0 / HEAD)
        m2 = _head_sum(dyn * yn, bd) * (1.0 / HEAD)
        dy_ref[...] = rstd * (dyn - m1 - yn * m2)
        dvb_ref[...] = dpre * rkk
        drkk = _head_sum(dpre * vv, bd)
        drb_ref[...] = drkk * kv * rk_v
        dkb_ref[...] = drkk * rv * rk_v
        drk_ref[...] += _colsum(drkk * rv * kv)

    row = lambda w: pl.BlockSpec((tm, w), lambda i: (i, 0))
    full = lambda a: pl.BlockSpec(a.shape, lambda i: (0, 0))
    once = lambda a: pl.BlockSpec(a.shape, lambda i: (0, 0), pipeline_mode=pl.Buffered(1))
    half = jax.ShapeDtypeStruct((s, D_HALF), F32)
    fshape = lambda a: jax.ShapeDtypeStruct(a.shape, F32)
    return pl.pallas_call(
        body, name="merge_fwd_bwd", grid=(s // tm,),
        in_specs=[row(D_HALF)] * 6 + [pl.BlockSpec((tm, D_HALF), lambda i: (i, 3)), row(d), row(d), row(d),
                                      once(w_g), once(wa), once(wb), once(wo), full(fg), full(lw), full(lb), full(rk)],
        out_specs=[row(d)] + [row(D_HALF)] * 7 + [row(GATE_COLS), full(wa), full(wb), full(wo), full(fg), full(fg),
                                                   full(lw), full(lb), full(rk)],
        out_shape=[jax.ShapeDtypeStruct((s, d), F32)] + [half] * 5 + [jax.ShapeDtypeStruct((s, D_HALF), BF16), half,
                                                                    jax.ShapeDtypeStruct((s, GATE_COLS), F32),
                                                                    fshape(wa), fshape(wb), fshape(wo), fshape(fg),
                                                                    fshape(fg), fshape(lw), fshape(lb), fshape(rk)],
        compiler_params=_params("arbitrary"),
    )(y, r, k, v, gate_a, o, u_b, h, x, tgt, w_g, wa, wb, wo, fg, lw, lb, rk)


def _lora_weight(w_up, a_up):
    z = jnp.zeros((LORA, D_HALF), w_up.dtype)
    return jnp.concatenate([jnp.concatenate([w_up, z], axis=1), jnp.concatenate([z, a_up], axis=1)], axis=0)


def _device_grads(x, tgt, p, w_a, w_up, a_up, late_weights, fwd_exchange=None, bwd_exchange=None, tail_exchange=None):
    wl = _lora_weight(w_up, a_up)
    rk = p["r_k"].reshape(1, D_HALF)
    fb = jnp.pad(p["f_bias"], ((0, 0), (0, LANES - N_HEADS)))
    qg = jnp.tile(p["q_norm_g"], (1, N_HEADS))
    kg = jnp.tile(p["k_norm_g"], (1, N_HEADS))
    fg = p["final_norm_g"].reshape(1, D_MODEL)
    mixer = (p["shift_mu"], wl, p["w0"], p["a0"], p["k_k"], p["k_a"])

    h, u_a, r, dec, k, v, av, bv, gate_a = _rwkv_front(x, p["norm_g"], w_a, *mixer)
    y, st, arrived = _wkv_fwd(r, dec, k, av, bv, v, fwd_exchange)

    w_b, w_g, w_out_a, w_out_b, w_out = late_weights(arrived)
    u_b, q, kn, vb, cc, cr = _fox_front(h, w_b, fb, qg, kg)
    o, lse = _attn_fwd(q, kn, vb, cc, cr)

    (dx2, dy, dr_b, dk_b, dv_b, dgate_a, do, dgate_b, du_g, dwa, dwb, dwo, dfg, loss_vec, dlw, dlb, drk) = _merge(
        y, r, k, v, gate_a, o, u_b, h, x, tgt, w_g, w_out_a, w_out_b, w_out, fg, p["lnx_w"], p["lnx_b"], rk)

    dd = _attn_bwd_rowdot(q, kn, vb, do, lse, cc, cr)
    dq, dk_att, dv_att, dcr = _attn_bwd(q, kn, vb, do, lse, dd, cc, cr)
    dcum = jnp.pad(dcr.T, ((0, 0), (0, LANES - N_HEADS)))
    h_t = h.T
    du_b, dw_b, dqg, dkg, dfb = _fox_prep_bwd(u_b, h_t, dq, dk_att, dv_att, dgate_b, dcum, fb, qg, kg)
    dw_g = _matmul_tn_acc(h_t, du_g, "dw_gate")

    scan_grads, sent = _wkv_bwd(r, dec, k, av, bv, v, dy, st,
                                bwd_exchange(dw_b, dw_g, dwa, dwb, dwo) if bwd_exchange else None)
    du_a, dw_a, dmu, dwl, dw0, da0, dkkw, dkaw = _rwkv_prep_bwd(
        u_a, h_t, (*scan_grads, dr_b, dk_b, dv_b, dgate_a), *mixer)
    dw_up, da_up = dwl[:LORA, :D_HALF], dwl[LORA:, D_HALF:]
    sent_last = _run_on_sequencer(tail_exchange(dw_a, dw_up, da_up), "scatter_tail", 1) if tail_exchange else []
    grad_x, dnorm_g, _ = _inproj_bwd(du_a, du_b, du_g, w_a, w_b, w_g, x, dx2, p["norm_g"])

    grads = dict(
        norm_g=dnorm_g, w_in=(dw_a, dw_b, dw_g), shift_mu=dmu,
        w_lora_up=dw_up, w0=dw0, a_lora_up=da_up, a0=da0, k_k=dkkw, k_a=dkaw,
        r_k=drk.reshape(1, N_HEADS, HEAD), lnx_w=dlw, lnx_b=dlb, f_bias=dfb[:, :N_HEADS],
        q_norm_g=dqg.reshape(N_HEADS, HEAD).sum(axis=0, keepdims=True),
        k_norm_g=dkg.reshape(N_HEADS, HEAD).sum(axis=0, keepdims=True),
        w_out_a=dwa, w_out_b=dwb, w_out=dwo, final_norm_g=dfg.reshape(D_MODEL))
    return loss_vec, grad_x, grads, sent, sent_last


CHIP_FLIPS = ((1, 0), (0, 1), (1, 1))
ANY = pl.BlockSpec(memory_space=pl.ANY)


def _position():
    return lax.axis_index("x"), lax.axis_index("y"), lax.axis_index("c")


def _flip(v, f):
    return 1 - v if f else v


def _both(a, b):
    if a is None:
        return b
    return a if b is None else jnp.logical_and(a, b)


def _when(cond, fn):
    if cond is None:
        fn()
    else:
        pl.when(cond)(fn)


class _Moves:
    def __init__(self, send_sems, recv_sems, local_sems):
        self.send_sems, self.recv_sems, self.local_sems = send_sems, recv_sems, local_sems
        self.remote, self.local = [], []

    def send(self, src, dst, peer, landing, send_if=None, recv_if=None, first=False):
        k = len(self.remote)
        sems = dict(send_sem=self.send_sems.at[k], recv_sem=self.recv_sems.at[k], device_id=peer, device_id_type=MESH)
        out = pltpu.make_async_remote_copy(src_ref=src, dst_ref=dst, **sems)
        arrival = pltpu.make_async_remote_copy(src_ref=src, dst_ref=landing, **sems)
        self.remote.append((out, arrival, send_if, recv_if, first))

    def copy(self, src, dst, cond=None):
        cp = pltpu.make_async_copy(src, dst, self.local_sems.at[len(self.local)])
        self.local.append((cp, cond))

    def start(self, also=None):
        for cp, cond in self.local:
            _when(_both(also, cond), cp.start)
        for out, _, send_if, _, _ in self.remote:
            _when(_both(also, send_if), out.start)

    def wait_arrivals(self, also=None, first=None):
        for _, arrival, _, recv_if, is_first in self.remote:
            if first is None or first == is_first:
                _when(_both(also, recv_if), arrival.wait_recv)

    def wait_sent(self, also=None):
        for out, _, send_if, _, _ in self.remote:
            _when(_both(also, send_if), out.wait_send)
        for cp, cond in self.local:
            _when(_both(also, cond), cp.wait)

    def wait(self, also=None):
        self.wait_arrivals(also)
        self.wait_sent(also)


class _Exchange:
    def __init__(self, operands, out_shapes, n_remote, n_local, build, relays=None, in_place=(), n_staging=0):
        self.operands, self.out_shapes = list(operands), list(out_shapes)
        self.n_remote, self.n_local, self.build = n_remote, n_local, build
        self.relays, self.in_place = relays, in_place
        self.n_staging = n_staging

    def scratch(self):
        return [pltpu.SemaphoreType.DMA((self.n_remote,)), pltpu.SemaphoreType.DMA((self.n_remote,)),
                pltpu.SemaphoreType.DMA((max(self.n_local, 1),))]

    def moves(self, in_refs, out_refs, sems):
        mv = _Moves(*sems)
        self.build(mv, in_refs, out_refs)
        return mv


def _run_on_sequencer(exchange, name, collective_id):
    ins = [jax.new_ref(a, memory_space=pltpu.MemorySpace.HBM) for a in exchange.operands]
    outs = [ins[i] if i in exchange.in_place else jax.empty_ref(s, memory_space=pltpu.MemorySpace.HBM)
            for i, s in enumerate(exchange.out_shapes)]
    forward, to_sibling = exchange.relays or (None, None)
    relay_scratch = [pltpu.SemaphoreType.DMA((stage[0],)) for stage in (forward, to_sibling) if stage for _ in range(2)]

    def launch(*sems):
        x, y, c = _position()
        peers = [(_flip(x, fx), _flip(y, fy), c) for fx, fy in CHIP_FLIPS] + ([(x, y, 1 - c)] if to_sibling else [])
        barrier = pltpu.get_barrier_semaphore()
        for peer in peers:
            pl.semaphore_signal(barrier, inc=1, device_id=peer, device_id_type=MESH)
        pl.semaphore_wait(barrier, len(peers))
        moves = exchange.moves(ins, outs, sems[:3])
        moves.start()
        later = []
        if forward:
            onward = _Moves(sems[3], sems[4], None)
            forward[1](onward, ins, outs)
            moves.wait_arrivals(first=True)
            onward.start()
            moves.wait_arrivals(first=False)
            onward.wait_arrivals()
            later.append(onward)
        else:
            moves.wait_arrivals()
        if to_sibling:
            passed = _Moves(*sems[-2:], None)
            to_sibling[1](passed, ins, outs)
            passed.start()
            passed.wait_arrivals()
            later.append(passed)
        for mv in later + [moves]:
            mv.wait_sent()

    pl.kernel(launch, mesh=plsc.ScalarSubcoreMesh(axis_name="sequencer", num_cores=1), name=name,
              scratch_types=tuple(exchange.scratch() + relay_scratch),
              compiler_params=pltpu.CompilerParams(collective_id=collective_id))()
    return [o[...] for o in outs[:len(outs) - exchange.n_staging]]


def _row_major_copy(a, name):
    r, c = a.shape
    tr = _row_tile(r)

    def body(a_ref, o_ref):
        o_ref[...] = a_ref[...]

    blk = pl.BlockSpec((tr, c), lambda i: (i, 0))
    return pl.pallas_call(body, name=name, grid=(r // tr,), in_specs=[blk], out_specs=blk,
                          out_shape=jax.ShapeDtypeStruct(a.shape, a.dtype), compiler_params=_params("parallel"))(a)


def _is_chip(x, y, chip):
    return jnp.logical_and(x == chip // 2, y == chip % 2)


def _gather_exchange(from_chip, from_all, split=()):
    n1, n2 = len(from_chip), len(from_all)
    near = CHIP_FLIPS[:2]

    def quarters(t, c, first, count=1):
        n = from_chip[t][1].shape[0] // 4
        return pl.ds((2 * c + first) * n, count * n)

    def build(mv, ins, outs):
        x, y, c = _position()
        me = 2 * x + y
        for t, (chip, _) in enumerate(from_chip):
            if t not in split:
                mv.copy(ins[t], outs[t], cond=_is_chip(x, y, chip))
        for t in range(n2):
            mv.copy(ins[n1 + t], outs[n1 + t].at[me])
        for t in split:
            for first in (True, False):
                for f, (fx, fy) in enumerate(near):
                    px, py = _flip(x, fx), _flip(y, fy)
                    part = quarters(t, c, f if first else 1 - f)
                    mv.send(ins[t].at[part], outs[t].at[part], (px, py, c), landing=outs[t].at[part], first=first,
                            send_if=_is_chip(x, y, from_chip[t][0]), recv_if=_is_chip(px, py, from_chip[t][0]))
        for fx, fy in CHIP_FLIPS:
            px, py = _flip(x, fx), _flip(y, fy)
            peer = (px, py, c)
            for t, (chip, _) in enumerate(from_chip):
                if t not in split:
                    mv.send(ins[t], outs[t], peer, landing=outs[t],
                            send_if=_is_chip(x, y, chip), recv_if=_is_chip(px, py, chip))
            for t in range(n2):
                mv.send(ins[n1 + t], outs[n1 + t].at[me], peer, landing=outs[n1 + t].at[2 * px + py])

    def forward(mv, ins, outs):
        x, y, c = _position()
        for t in split:
            chip = from_chip[t][0]
            for f, (fx, fy) in enumerate(near):
                gx, gy = near[1 - f]
                part = quarters(t, c, f)
                mv.send(outs[t].at[part], outs[t].at[part], (_flip(x, gx), _flip(y, gy), c), landing=outs[t].at[part],
                        send_if=_is_chip(_flip(x, fx), _flip(y, fy), chip), recv_if=_is_chip(1 - x, 1 - y, chip))

    def to_sibling(mv, ins, outs):
        x, y, c = _position()
        for t in split:
            came = jnp.logical_not(_is_chip(x, y, from_chip[t][0]))
            mv.send(outs[t].at[quarters(t, c, 0, 2)], outs[t].at[quarters(t, c, 0, 2)], (x, y, 1 - c),
                    landing=outs[t].at[quarters(t, 1 - c, 0, 2)], send_if=came, recv_if=came)

    arrays = [a for _, a in from_chip] + list(from_all)
    shapes = [jax.ShapeDtypeStruct(a.shape, a.dtype) for _, a in from_chip]
    shapes += [jax.ShapeDtypeStruct((N_CHIPS,) + a.shape, a.dtype) for a in from_all]
    n_remote = len(CHIP_FLIPS) * (n1 - len(split) + n2) + 2 * len(near) * len(split)
    relays = ((len(near) * len(split), forward), (len(split), to_sibling)) if split else None
    return _Exchange(arrays, shapes, n_remote, n1 + n2, build, relays, in_place=split)


def _scatter_exchange(to_chip, to_all, via_neighbours=False):
    n1, n2 = len(to_chip), len(to_all)
    near = CHIP_FLIPS[:2]
    direct = near if via_neighbours else CHIP_FLIPS

    def half(t, g):
        n = to_chip[t][1].shape[0] // 2
        return pl.ds(g * n, n)

    def build(mv, ins, outs):
        x, y, c = _position()
        if via_neighbours:
            for t, (chip, _) in enumerate(to_chip):
                for g, (gx, gy) in enumerate(near):
                    ox, oy = near[1 - g]
                    mv.send(ins[t].at[half(t, g)], outs[n1 + n2 + t], (_flip(x, gx), _flip(y, gy), c),
                            landing=outs[n1 + n2 + t], first=True, send_if=_is_chip(1 - x, 1 - y, chip),
                            recv_if=_is_chip(_flip(x, ox), _flip(y, oy), chip))
        for f, (fx, fy) in enumerate(CHIP_FLIPS):
            px, py = _flip(x, fx), _flip(y, fy)
            peer = (px, py, c)
            if (fx, fy) in direct:
                for t, (chip, _) in enumerate(to_chip):
                    mv.send(ins[t], outs[t].at[f], peer, landing=outs[t].at[f],
                            send_if=_is_chip(px, py, chip), recv_if=_is_chip(x, y, chip))
            for t in range(n2):
                mv.send(ins[n1 + t].at[2 * px + py], outs[n1 + t].at[f], peer, landing=outs[n1 + t].at[f])

    def forward(mv, ins, outs):
        x, y, c = _position()
        for t, (chip, _) in enumerate(to_chip):
            for g in range(len(near)):
                ox, oy = near[1 - g]
                far_slot = outs[t].at[len(near)].at[half(t, g)]
                mv.send(outs[n1 + n2 + t], far_slot, (_flip(x, ox), _flip(y, oy), c), landing=far_slot,
                        send_if=_is_chip(_flip(x, ox), _flip(y, oy), chip), recv_if=_is_chip(x, y, chip))

    arrays = [a for _, a in to_chip] + list(to_all)
    shapes = [jax.ShapeDtypeStruct((len(CHIP_FLIPS),) + a.shape, a.dtype) for _, a in to_chip]
    shapes += [jax.ShapeDtypeStruct((len(CHIP_FLIPS),) + a.shape[1:], a.dtype) for a in to_all]
    if not via_neighbours:
        return _Exchange(arrays, shapes, len(CHIP_FLIPS) * (n1 + n2), 0, build)
    shapes += [jax.ShapeDtypeStruct((a.shape[0] // 2, a.shape[1]), a.dtype) for _, a in to_chip]
    return _Exchange(arrays, shapes, 2 * len(near) * n1 + len(CHIP_FLIPS) * n2, 0, build,
                     relays=((len(near) * n1, forward), None), n_staging=n1)


def _swap_sibling(tensors, name):
    n = len(tensors)

    def body(*refs):
        ins, outs = refs[:n], refs[n:2 * n]
        send_sems, recv_sems = refs[2 * n:]
        x, y, c = _position()
        copies = [pltpu.make_async_remote_copy(
            src_ref=ins[t], dst_ref=outs[t], send_sem=send_sems.at[t], recv_sem=recv_sems.at[t],
            device_id=(x, y, 1 - c), device_id_type=MESH) for t in range(n)]
        for cp in copies:
            cp.start()
        for cp in copies:
            cp.wait_recv()
        for cp in copies:
            cp.wait_send()

    return pl.pallas_call(
        body, name=name, in_specs=[ANY] * n, out_specs=[ANY] * n,
        out_shape=[jax.ShapeDtypeStruct(a.shape, a.dtype) for a in tensors],
        scratch_shapes=[pltpu.SemaphoreType.DMA((n,)), pltpu.SemaphoreType.DMA((n,))],
        compiler_params=pltpu.CompilerParams(has_side_effects=True),
    )(*tensors)


def _allreduce_small(slab):
    stages = 3

    def body(x_ref, o_ref, buf, send_sems, recv_sems):
        x, y, c = _position()
        peers = ((1 - x, y, c), (x, 1 - y, c), (x, y, 1 - c))
        o_ref[...] = x_ref[...]
        for k, peer in enumerate(peers):
            cp = pltpu.make_async_remote_copy(src_ref=o_ref, dst_ref=buf.at[k], send_sem=send_sems.at[k],
                                              recv_sem=recv_sems.at[k], device_id=peer, device_id_type=MESH)
            cp.start()
            cp.wait()
            o_ref[...] = o_ref[...] + buf[k]

    return pl.pallas_call(
        body, name="allreduce_small",
        in_specs=[pl.BlockSpec(memory_space=pltpu.VMEM)], out_specs=pl.BlockSpec(memory_space=pltpu.VMEM),
        out_shape=jax.ShapeDtypeStruct(slab.shape, slab.dtype),
        scratch_shapes=[pltpu.VMEM((stages,) + slab.shape, slab.dtype),
                        pltpu.SemaphoreType.DMA((stages,)), pltpu.SemaphoreType.DMA((stages,))],
        compiler_params=pltpu.CompilerParams(has_side_effects=True),
    )(slab)


def _row_tile(r):
    return min(r, 256)


def _sum4(stack, recv, me):
    _, r, c = stack.shape
    tr = _row_tile(r)

    def body(me_ref, own_ref, recv_ref, o_ref):
        o_ref[...] = (((own_ref[...] + recv_ref[0].astype(F32)) + recv_ref[1].astype(F32))
                      + recv_ref[2].astype(F32))

    return pl.pallas_call(
        body, name="sum_partials",
        grid_spec=pltpu.PrefetchScalarGridSpec(
            num_scalar_prefetch=1, grid=(r // tr,),
            in_specs=[pl.BlockSpec((None, tr, c), lambda i, me_ref: (me_ref[0], i, 0)),
                      pl.BlockSpec((len(CHIP_FLIPS), tr, c), lambda i, me_ref: (0, i, 0))],
            out_specs=pl.BlockSpec((tr, c), lambda i, me_ref: (i, 0))),
        out_shape=jax.ShapeDtypeStruct((r, c), F32), compiler_params=_params("parallel"),
    )(me, stack, recv)


def _sum_block(own, recv):
    r, c = own.shape
    tr = _row_tile(r)

    def body(own_ref, recv_ref, o_ref):
        o_ref[...] = (((own_ref[...] + recv_ref[0].astype(F32)) + recv_ref[1].astype(F32))
                      + recv_ref[2].astype(F32))

    return pl.pallas_call(
        body, name="sum_block", grid=(r // tr,),
        in_specs=[pl.BlockSpec((tr, c), lambda i: (i, 0)), pl.BlockSpec((len(CHIP_FLIPS), tr, c), lambda i: (0, i, 0))],
        out_specs=pl.BlockSpec((tr, c), lambda i: (i, 0)),
        out_shape=jax.ShapeDtypeStruct((r, c), F32), compiler_params=_params("parallel"),
    )(own, recv)


def _adam_m(g, m):
    return ADAM_B1 * m + (1.0 - ADAM_B1) * g


def _adam_v(g, v):
    return ADAM_B2 * v + (1.0 - ADAM_B2) * (g * g)


def _adamw_math(w, g, m, v):
    m = _adam_m(g, m)
    v = _adam_v(g, v)
    m_hat = m / (1.0 - ADAM_B1 ** ADAM_STEP)
    v_hat = v / (1.0 - ADAM_B2 ** ADAM_STEP)
    delta = -ADAM_LR * (m_hat / (jnp.sqrt(v_hat) + ADAM_EPS) + ADAM_WD * w)
    return delta, m, v


def _adamw(w, m, v, g_parts, name):
    r, c = w.shape
    tr = _row_tile(r)
    n = len(g_parts)

    def body(*refs):
        w_ref, m_ref, v_ref = refs[:3]
        g_refs = refs[3:3 + n]
        g_out, d_out, m_out, v_out = refs[3 + n:]
        g = g_refs[0][...]
        for ref in g_refs[1:]:
            g = g + ref[...]
        g_out[...] = g
        d_out[...], m_out[...], v_out[...] = _adamw_math(w_ref[...], g, m_ref[...], v_ref[...])

    blk = pl.BlockSpec((tr, c), lambda i: (i, 0))
    return pl.pallas_call(
        body, name=name, grid=(r // tr,), in_specs=[blk] * (3 + n), out_specs=[blk] * 4,
        out_shape=[jax.ShapeDtypeStruct((r, c), F32)] * 4, compiler_params=_params("parallel"),
    )(w, m, v, *g_parts)


def _adamw_staged(w, m, v, g_parts, name):
    r, c = w.shape
    blk = pl.BlockSpec((_row_tile(r), c), lambda i: (i, 0))

    def stage(body, suffix, *operands):
        return pl.pallas_call(body, name=name + suffix, grid=(r // _row_tile(r),), in_specs=[blk] * len(operands),
                              out_specs=blk, out_shape=jax.ShapeDtypeStruct((r, c), F32),
                              compiler_params=_params("parallel"))(*operands)

    def sum_body(*refs):
        g = refs[0][...]
        for ref in refs[1:-1]:
            g = g + ref[...]
        refs[-1][...] = g

    def m_body(g_ref, m_ref, o_ref):
        o_ref[...] = _adam_m(g_ref[...], m_ref[...])

    def v_body(g_ref, v_ref, o_ref):
        o_ref[...] = _adam_v(g_ref[...], v_ref[...])

    def delta_body(w_ref, g_ref, m_ref, v_ref, o_ref):
        o_ref[...] = _adamw_math(w_ref[...], g_ref[...], m_ref[...], v_ref[...])[0]

    g = stage(sum_body, "_grad", *g_parts)
    new_m = stage(m_body, "_m", g, m)
    new_v = stage(v_body, "_v", g, v)
    return g, stage(delta_body, "_delta", w, g, m, v), new_m, new_v


def _adamw_small(total, w, m, v):
    sizes = [w[n].size for n in SMALL]
    flat = lambda d: [d[n].reshape(1, -1) for n in SMALL]
    k = len(SMALL)

    def body(*refs):
        total_ref, w_refs, m_refs, v_refs = refs[0], refs[1:1 + k], refs[1 + k:1 + 2 * k], refs[1 + 2 * k:1 + 3 * k]
        outs = refs[1 + 3 * k:]
        for i, size in enumerate(sizes):
            g = total_ref[i:i + 1, 0:size]
            outs[i][...] = g
            outs[k + i][...], outs[2 * k + i][...], outs[3 * k + i][...] = _adamw_math(
                w_refs[i][...], g, m_refs[i][...], v_refs[i][...])

    res = pl.pallas_call(
        body, name="adamw_small", out_shape=[jax.ShapeDtypeStruct((1, size), F32) for size in sizes] * 4,
        compiler_params=_params(),
    )(total, *flat(w), *flat(m), *flat(v))
    return [{n: res[j * k + i].reshape(w[n].shape) for i, n in enumerate(SMALL)} for j in range(4)]


SHARDED = ("w_in", "w_lora_up", "a_lora_up", "w_out_a", "w_out_b", "w_out")
ROW_SHARDED = ("w_out",)
SMALL = ("norm_g", "shift_mu", "w0", "a0", "k_k", "k_a", "r_k", "lnx_w", "lnx_b", "f_bias", "q_norm_g", "k_norm_g",
         "final_norm_g")
WEIGHTS = ("norm_g", "w_in", "shift_mu", "w_lora_up", "w0", "a_lora_up", "a0", "k_k", "k_a", "r_k", "lnx_w", "lnx_b",
           "f_bias", "q_norm_g", "k_norm_g", "w_out_a", "w_out_b", "w_out", "final_norm_g")
SLAB_ROWS = 16
SLAB_COLS = SEC


def _to_slab(named, extra=None):
    rows = [jnp.pad(named[n].reshape(1, -1), ((0, 0), (0, SLAB_COLS - named[n].size))) for n in SMALL]
    if extra is not None:
        rows.append(jnp.pad(extra.reshape(1, -1), ((0, 0), (0, SLAB_COLS - extra.size))))
    rows.append(jnp.zeros((SLAB_ROWS - len(rows), SLAB_COLS), F32))
    return jnp.concatenate(rows, axis=0)


def _by_chip(g, name):
    if name in ROW_SHARDED:
        return g.reshape(N_CHIPS, g.shape[0] // N_CHIPS, g.shape[1])
    r, c = g.shape
    return g.reshape(r, N_CHIPS, c // N_CHIPS).transpose(1, 0, 2)


def _from_chips(stack, name):
    if name in ROW_SHARDED:
        return stack.reshape(-1, stack.shape[2])
    _, r, c = stack.shape
    return stack.transpose(1, 0, 2).reshape(r, N_CHIPS * c)


def kernel(x, norm_g, w_in, shift_mu, w_lora_up, w0, a_lora_up, a0, k_k, k_a, r_k, lnx_w, lnx_b, f_bias, q_norm_g, k_norm_g, w_out_a, w_out_b, w_out, final_norm_g, loss_target, m_norm_g, m_w_in, m_shift_mu, m_w_lora_up, m_w0, m_a_lora_up, m_a0, m_k_k, m_k_a, m_r_k, m_lnx_w, m_lnx_b, m_f_bias, m_q_norm_g, m_k_norm_g, m_w_out_a, m_w_out_b, m_w_out, m_final_norm_g, v_norm_g, v_w_in, v_shift_mu, v_w_lora_up, v_w0, v_a_lora_up, v_a0, v_k_k, v_k_a, v_r_k, v_lnx_w, v_lnx_b, v_f_bias, v_q_norm_g, v_k_norm_g, v_w_out_a, v_w_out_b, v_w_out, v_final_norm_g):
    w = dict(norm_g=norm_g, w_in=w_in, shift_mu=shift_mu, w_lora_up=w_lora_up, w0=w0, a_lora_up=a_lora_up, a0=a0,
             k_k=k_k, k_a=k_a, r_k=r_k, lnx_w=lnx_w, lnx_b=lnx_b, f_bias=f_bias, q_norm_g=q_norm_g,
             k_norm_g=k_norm_g, w_out_a=w_out_a, w_out_b=w_out_b, w_out=w_out, final_norm_g=final_norm_g)
    m = dict(norm_g=m_norm_g, w_in=m_w_in, shift_mu=m_shift_mu, w_lora_up=m_w_lora_up, w0=m_w0,
             a_lora_up=m_a_lora_up, a0=m_a0, k_k=m_k_k, k_a=m_k_a, r_k=m_r_k, lnx_w=m_lnx_w, lnx_b=m_lnx_b,
             f_bias=m_f_bias, q_norm_g=m_q_norm_g, k_norm_g=m_k_norm_g, w_out_a=m_w_out_a, w_out_b=m_w_out_b,
             w_out=m_w_out, final_norm_g=m_final_norm_g)
    v = dict(norm_g=v_norm_g, w_in=v_w_in, shift_mu=v_shift_mu, w_lora_up=v_w_lora_up, w0=v_w0,
             a_lora_up=v_a_lora_up, a0=v_a0, k_k=v_k_k, k_a=v_k_a, r_k=v_r_k, lnx_w=v_lnx_w, lnx_b=v_lnx_b,
             f_bias=v_f_bias, q_norm_g=v_q_norm_g, k_norm_g=v_k_norm_g, w_out_a=v_w_out_a, w_out_b=v_w_out_b,
             w_out=v_w_out, final_norm_g=v_final_norm_g)
    shapes = {n: w[n].shape for n in WEIGHTS}

    shard = {n: w[n][0].astype(BF16) for n in SHARDED}
    late = ("w_out_a", "w_out_b", "w_out")
    loras = ("w_lora_up", "a_lora_up")
    w_in_head, w_in_tail = shard["w_in"][:, :A_TAIL], shard["w_in"][:, A_TAIL:]
    shard0, shard1_head, up_stack, aup_stack = _run_on_sequencer(_gather_exchange(
        [(0, shard["w_in"]), (1, w_in_head)], [shard[n] for n in loras], split=(0, 1)), "gather_early", 2)
    moments = (_row_major_copy(m["w_in"][0], "m_w_in_rows"), _row_major_copy(v["w_in"][0], "v_w_in_rows"))
    shard0, moments = lax.optimization_barrier((shard0, moments))
    w_a = jnp.concatenate([shard0, shard1_head], axis=1)

    def late_weights(arrived):
        shard1_tail, shard2, shard3 = arrived[:3]
        w_b = jnp.concatenate([shard1_tail, shard2[:, :B_TAIL], jnp.zeros((D_MODEL, SEC - FOX_REAL), BF16)], axis=1)
        w_g = jnp.concatenate([shard2[:, B_TAIL:], shard3], axis=1)
        return (w_b, w_g, *[_from_chips(s, n) for n, s in zip(late, arrived[3:])])

    own = {}

    def bwd_exchange(dw_b, dw_g, dwa, dwb, dwo):
        own["tail1"] = dw_b[:, :B_HEAD]
        own["block2"] = jnp.concatenate([dw_b[:, B_HEAD:FOX_REAL], dw_g[:, :G_HEAD]], axis=1)
        own["block3"] = dw_g[:, G_HEAD:]
        own.update({n: _by_chip(g, n) for n, g in zip(late, (dwa, dwb, dwo))})
        return _scatter_exchange([(1, own["tail1"].astype(BF16)), (2, own["block2"].astype(BF16)),
                                  (3, own["block3"].astype(BF16))], [own[n].astype(BF16) for n in late])

    def tail_exchange(dw_a, dw_up, da_up):
        own["block0"], own["head1"] = dw_a[:, :SHARD_COLS], dw_a[:, SHARD_COLS:]
        own.update({n: _by_chip(g, n) for n, g in zip(loras, (dw_up, da_up))})
        return _scatter_exchange([(0, own["block0"].astype(BF16)), (1, own["head1"].astype(BF16))],
                                 [own[n].astype(BF16) for n in loras], via_neighbours=True)

    small = {n: w[n] for n in SMALL}
    loss_vec, grad_x, grads, sent, sent_last = _device_grads(
        x[0], loss_target[0], small, w_a, _from_chips(up_stack, "w_lora_up"), _from_chips(aup_stack, "a_lora_up"),
        late_weights, _gather_exchange([(1, w_in_tail), (2, shard["w_in"]), (3, shard["w_in"])], [shard[n] for n in late]),
        bwd_exchange, tail_exchange)

    total = _allreduce_small(_to_slab(grads, extra=loss_vec))
    loss = (0.5 / D_MODEL) * jnp.sum(total[len(SMALL)])
    out_g, out_d, out_m, out_v = _adamw_small(total, w, m, v)

    xpos, ypos, _ = _position()
    me = (2 * xpos + ypos).astype(jnp.int32).reshape(1)
    core_sum = {n: _sum4(own[n], r, me) for n, r in zip(late, sent[3:])}
    theirs = dict(zip(late, _swap_sibling([core_sum[n] for n in late], "swap_sibling_early")))

    def update(n):
        m_n, v_n = moments if n == "w_in" else (m[n][0], v[n][0])
        step = _adamw_staged if n == "w_in" else _adamw
        g, d, m2, v2 = step(w[n][0], m_n, v_n, [core_sum[n], theirs[n]], "adamw_" + n)
        out_g[n], out_d[n], out_m[n], out_v[n] = (a.reshape(shapes[n]) for a in (g, d, m2, v2))

    for n in late:
        update(n)
    done = (out_d["norm_g"], [out_d[n] for n in late])
    sent_last, (out_d["norm_g"], new_d) = lax.optimization_barrier((sent_last, done))
    out_d.update(zip(late, new_d))
    core_sum["w_in"] = lax.switch(me[0], [
        lambda: _sum_block(own["block0"], sent_last[0]),
        lambda: jnp.concatenate([_sum_block(own["head1"], sent_last[1]), _sum_block(own["tail1"], sent[0])], axis=1),
        lambda: _sum_block(own["block2"], sent[1]),
        lambda: _sum_block(own["block3"], sent[2])])
    core_sum.update({n: _sum4(own[n], r, me) for n, r in zip(loras, sent_last[2:])})
    rest = ("w_in",) + loras
    theirs.update(zip(rest, _swap_sibling([core_sum[n] for n in rest], "swap_sibling")))
    for n in rest:
        update(n)

    return (loss, grad_x.reshape(x.shape), *[out_g[n] for n in WEIGHTS], *[out_d[n] for n in WEIGHTS],
            *[out_m[n] for n in WEIGHTS], *[out_v[n] for n in WEIGHTS])
```

```python
import functools
import math

import jax
import jax.numpy as jnp
from jax import lax
from jax.experimental import pallas as pl
from jax.experimental.pallas import tpu as pltpu
from jax.experimental.pallas import tpu_sc as plsc

F32 = jnp.float32
BF16 = jnp.bfloat16

D_MODEL = 1024
D_HALF = 512
HEAD = 64
N_HEADS = 8
LORA = 64
RWKV_COLS = 2176
FOX_REAL = 2056
SEC = 2176
GATE_COLS = 2048
IN_COLS = 6280
N_CHIPS = 4
SHARD_COLS = IN_COLS // N_CHIPS
A_TAIL = RWKV_COLS - SHARD_COLS
B_HEAD = SHARD_COLS - A_TAIL
B_TAIL = FOX_REAL - B_HEAD
G_HEAD = SHARD_COLS - B_TAIL
RMS_EPS = 1e-6
LNX_EPS = 64e-5
ATT_SCALE = HEAD ** -0.5
NEG = -1e30

ADAM_LR = 0.001
ADAM_B1 = 0.9
ADAM_B2 = 0.999
ADAM_EPS = 1e-08
ADAM_WD = 0.01
ADAM_STEP = 10

LANES = 128
SUBLANES = 8
VMEM_LIMIT = 56 * 1024 * 1024
MESH = pl.DeviceIdType.MESH


def _params(*sem):
    return pltpu.CompilerParams(dimension_semantics=sem if sem else None, vmem_limit_bytes=VMEM_LIMIT)


def _sigmoid(x):
    return 1.0 / (1.0 + jnp.exp(-x))


def _log_sigmoid(x):
    return jnp.minimum(x, 0.0) - jnp.log(1.0 + jnp.exp(-jnp.abs(x)))


def _head_ones():
    r = lax.broadcasted_iota(jnp.int32, (LANES, LANES), 0) >> 6
    c = lax.broadcasted_iota(jnp.int32, (LANES, LANES), 1) >> 6
    return (r == c).astype(BF16)


def _split3(x):
    hi = x.astype(BF16)
    r1 = x - hi.astype(F32)
    mid = r1.astype(BF16)
    lo = (r1 - mid.astype(F32)).astype(BF16)
    return hi, mid, lo


def _exact_dot(x, ones_bf16, ones_first=False):
    out = None
    for piece in _split3(x):
        if ones_first:
            t = jnp.dot(ones_bf16, piece, preferred_element_type=F32)
        else:
            t = jnp.dot(piece, ones_bf16, preferred_element_type=F32)
        out = t if out is None else out + t
    return out


def _head_sum(x, bd):
    n = x.shape[1] // LANES
    parts = [_exact_dot(x[:, i * LANES:(i + 1) * LANES], bd) for i in range(n)]
    return parts[0] if n == 1 else jnp.concatenate(parts, axis=1)


def _dot_nt(a, b):
    return lax.dot_general(a, b, (((1,), (1,)), ((), ())), preferred_element_type=F32)


def _dot_tn(a, b):
    return lax.dot_general(a, b, (((0,), (0,)), ((), ())), preferred_element_type=F32)


def _colsum(x):
    return jnp.sum(x, axis=0, keepdims=True)


def _matmul_tn_acc(at, b, name, tk=512):
    m, k = at.shape
    n = b.shape[1]

    def body(a_ref, b_ref, o_ref):
        j = pl.program_id(0)

        @pl.when(j == 0)
        def _():
            o_ref[...] = jnp.zeros_like(o_ref)

        o_ref[...] += jnp.dot(a_ref[...], b_ref[...].astype(BF16), preferred_element_type=F32)

    return pl.pallas_call(
        body, name=name, grid=(k // tk,),
        in_specs=[pl.BlockSpec((m, tk), lambda j: (0, j)), pl.BlockSpec((tk, n), lambda j: (j, 0))],
        out_specs=pl.BlockSpec((m, n), lambda j: (0, 0)),
        out_shape=jax.ShapeDtypeStruct((m, n), F32), compiler_params=_params("arbitrary"),
    )(at, b)


def _inproj_bwd(du_a, du_b, du_g, w_a, w_b, w_g, x, dx2, g, exchange=None, tm=256):
    s, d = x.shape
    nb = s // tm

    def body(*refs):
        ((da_ref, db_ref, dg_ref, wa_ref, wb_ref, wg_ref, x_ref, dx2_ref, g_ref), (gx_ref, gg_ref), _,
         moves) = _split_refs(refs, 9, 2, exchange)
        i = pl.program_id(0)
        if moves:
            moves.start(also=(i == 0))

        @pl.when(i == 0)
        def _():
            gg_ref[...] = jnp.zeros_like(gg_ref)

        dh = _dot_nt(da_ref[...].astype(BF16), wa_ref[...])
        dh += _dot_nt(db_ref[...].astype(BF16), wb_ref[...])
        dh += _dot_nt(dg_ref[...].astype(BF16), wg_ref[...])
        xv = x_ref[...]
        r = lax.rsqrt(jnp.mean(xv * xv, axis=-1, keepdims=True) + RMS_EPS)
        xh = xv * r
        gg_ref[...] += _colsum(dh * xh)
        dxh = dh * g_ref[...]
        gx_ref[...] = dx2_ref[...] + r * (dxh - xh * jnp.mean(dxh * xh, axis=-1, keepdims=True))
        if moves:
            moves.wait(also=(i == nb - 1))

    row = lambda w: pl.BlockSpec((tm, w), lambda i: (i, 0))
    full = lambda a: pl.BlockSpec(a.shape, lambda i: (0, 0))
    ex_in = exchange.operands if exchange else []
    ex_out = exchange.out_shapes if exchange else []
    res = pl.pallas_call(
        body, name="inproj_bwd", grid=(nb,),
        in_specs=[row(SEC), row(SEC), row(GATE_COLS), full(w_a), full(w_b), full(w_g), row(d), row(d), full(g)]
                 + [ANY] * len(ex_in),
        out_specs=[row(d), pl.BlockSpec((1, d), lambda i: (0, 0))] + [ANY] * len(ex_out),
        out_shape=[jax.ShapeDtypeStruct((s, d), F32), jax.ShapeDtypeStruct((1, d), F32)] + ex_out,
        scratch_shapes=exchange.scratch() if exchange else [],
        compiler_params=_params("arbitrary"),
    )(du_a, du_b, du_g, w_a, w_b, w_g, x, dx2, g, *ex_in)
    return res[0], res[1], list(res[2:])


def _rwkv_elementwise(ua, prev_row, first, mu, wl, w0, a0, kkw, kaw, bd):
    tm = ua.shape[0]
    rows = lax.broadcasted_iota(jnp.int32, (tm, 1), 0)
    prev = jnp.where(first, jnp.zeros_like(prev_row), prev_row)
    shifted = jnp.where(rows == 0, prev, pltpu.roll(ua, 1, 0))
    delta = shifted - ua
    us = ua + delta * mu
    r = us[:, 0:512]
    k0 = us[:, 512:1024]
    v = us[:, 1024:1536]
    lo = us[:, 1536:1664]
    gate = us[:, 1664:2176]
    lane = lax.broadcasted_iota(jnp.int32, (1, LANES), 1)
    th = jnp.tanh(lo)
    lin = jnp.where(lane < LORA, th, lo)
    ll = jnp.dot(lin.astype(BF16), wl, preferred_element_type=F32)
    sz = _sigmoid(w0 + ll[:, :512])
    e = sz * math.exp(-0.5)
    dec = jnp.exp(-e)
    a = _sigmoid(a0 + ll[:, 512:])
    kk0 = k0 * kkw
    ss = _head_sum(kk0 * kk0, bd)
    nrm = jnp.maximum(jnp.sqrt(ss), 1e-12)
    kk = kk0 / nrm
    k = k0 * (1.0 + (a - 1.0) * kaw)
    return dict(delta=delta, us=us, r=r, k0=k0, v=v, lo=lo, gate=gate, th=th, lin=lin, sz=sz, e=e, dec=dec,
                a=a, kk0=kk0, ss=ss, nrm=nrm, kk=kk, k=k)


def _rwkv_front(x, g, w_a, mu, wl, w0, a0, kkw, kaw, tm=256):
    s, d = x.shape

    def body(x_ref, g_ref, wa_ref, mu_ref, wl_ref, w0_ref, a0_ref, kkw_ref, kaw_ref,
             h_ref, ua_ref, r_ref, w_ref, k_ref, v_ref, a_ref, b_ref, gate_ref, last_row):
        i = pl.program_id(0)
        xv = x_ref[...]
        h = (xv * lax.rsqrt(jnp.mean(xv * xv, axis=-1, keepdims=True) + RMS_EPS) * g_ref[...]).astype(BF16)
        h_ref[...] = h
        ua = jnp.dot(h, wa_ref[...], preferred_element_type=F32)
        ua_ref[...] = ua

        @pl.when(i == 0)
        def _():
            last_row[...] = jnp.zeros_like(last_row)

        f = _rwkv_elementwise(ua, last_row[...], i == 0, mu_ref[...], wl_ref[...], w0_ref[...],
                              a0_ref[...], kkw_ref[...], kaw_ref[...], _head_ones())
        last_row[...] = ua[tm - 1:tm, :]
        r_ref[...] = f["r"]
        w_ref[...] = f["dec"]
        k_ref[...] = f["k"]
        v_ref[...] = f["v"]
        a_ref[...] = -f["kk"]
        b_ref[...] = f["kk"] * f["a"]
        gate_ref[...] = f["gate"]

    vec = lambda w: pl.BlockSpec((1, w), lambda i: (0, 0))
    row = lambda w: pl.BlockSpec((tm, w), lambda i: (i, 0))
    return pl.pallas_call(
        body, name="rwkv_front", grid=(s // tm,),
        in_specs=[row(d), vec(d), pl.BlockSpec(w_a.shape, lambda i: (0, 0), pipeline_mode=pl.Buffered(1)),
                  vec(SEC), pl.BlockSpec((LANES, 2 * D_HALF), lambda i: (0, 0)),
                  vec(D_HALF), vec(D_HALF), vec(D_HALF), vec(D_HALF)],
        out_specs=[row(d), row(SEC)] + [row(D_HALF)] * 7,
        out_shape=[jax.ShapeDtypeStruct((s, d), BF16), jax.ShapeDtypeStruct((s, SEC), F32)]
                  + [jax.ShapeDtypeStruct((s, D_HALF), F32)] * 7,
        scratch_shapes=[pltpu.VMEM((1, SEC), F32)],
        compiler_params=_params("arbitrary"),
    )(x, g, w_a, mu, wl, w0, a0, kkw, kaw)


SCAN_TB = 128
N_PAIRS = 4


def _pair_sum(x, left):
    s_l = jnp.sum(jnp.where(left, x, 0.0), axis=1, keepdims=True)
    s_r = jnp.sum(jnp.where(left, 0.0, x), axis=1, keepdims=True)
    return jnp.where(left, s_l, s_r)


def _pair_dot(x, row_l, row_r, left):
    s_l = jnp.sum(x * row_l, axis=1, keepdims=True)
    s_r = jnp.sum(x * row_r, axis=1, keepdims=True)
    return jnp.where(left, s_l, s_r)


def _halves(rows8):
    lane = lax.broadcasted_iota(jnp.int32, rows8.shape, 1)
    keep_left = (lane & (LANES - 1)) < HEAD
    return jnp.where(keep_left, rows8, 0.0), jnp.where(keep_left, 0.0, rows8)


def _quad_consts():
    lane = lax.broadcasted_iota(jnp.int32, (HEAD, 2 * LANES), 1)
    rowi = lax.broadcasted_iota(jnp.int32, (HEAD, 2 * LANES), 0)
    diag2 = rowi == (lane & (HEAD - 1))
    r = lax.broadcasted_iota(jnp.int32, (2 * LANES, 2 * LANES), 0) >> 6
    c = lax.broadcasted_iota(jnp.int32, (2 * LANES, 2 * LANES), 1) >> 6
    return diag2, (r == c).astype(BF16)


def _rows_to_columns(x8, diag2, bd2):
    lhs = jnp.concatenate([jnp.where(diag2, x8[i:i + 1], 0.0).astype(BF16) for i in range(SUBLANES)], axis=0)
    return jnp.dot(lhs, bd2, preferred_element_type=F32)


def _diag_rows(qtile, diag2, bd2, sub_row2):
    res = jnp.dot(qtile, bd2, preferred_element_type=F32)
    out = jnp.zeros((SUBLANES, 2 * LANES), F32)
    for i in range(SUBLANES):
        out = jnp.where(sub_row2 == i, _colsum(jnp.where(diag2, res[i * HEAD:(i + 1) * HEAD], 0.0)), out)
    return out


def _store_tile(qbuf, slot, p, i, x):
    qbuf[slot, p // 2, i * HEAD:(i + 1) * HEAD, (p % 2) * LANES:(p % 2 + 1) * LANES] = x.astype(BF16)


def _left_half():
    return lax.broadcasted_iota(jnp.int32, (HEAD, LANES), 1) < HEAD


def _split_refs(refs, n_rows, n_out, exchange):
    n_in = len(exchange.operands) if exchange else 0
    n_ex_out = len(exchange.out_shapes) if exchange else 0
    refs = list(refs)
    rows, refs = refs[:n_rows], refs[n_rows:]
    ex_in, refs = refs[:n_in], refs[n_in:]
    outs, refs = refs[:n_out], refs[n_out:]
    ex_out, refs = refs[:n_ex_out], refs[n_ex_out:]
    scratch, sems = (refs[:-3], refs[-3:]) if exchange else (refs, None)
    moves = exchange.moves(ex_in, ex_out, sems) if exchange else None
    return rows, outs, scratch, moves


def _wkv_fwd(r, w, k, a, b, v, exchange=None):
    s = r.shape[0]
    tb = SCAN_TB
    nb = s // tb

    def body(*refs):
        (r_ref, w_ref, k_ref, a_ref, b_ref, v_ref), (y_ref, st_ref), (state, vbuf, qbuf), moves = _split_refs(
            refs, 6, 2, exchange)
        g = pl.program_id(0)
        if moves:
            moves.start(also=(g == 0))

        @pl.when(g == 0)
        def _():
            state[...] = jnp.zeros_like(state)
            qbuf[...] = jnp.zeros_like(qbuf)

        left = _left_half()
        diag2, bd2 = _quad_consts()
        sub_row2 = lax.broadcasted_iota(jnp.int32, (SUBLANES, 2 * LANES), 0)
        groups = tb // SUBLANES
        quads = [slice(g2 * 2 * LANES, (g2 + 1) * 2 * LANES) for g2 in range(2)]

        def rows_of(q):
            return pl.ds(pl.multiple_of(q * SUBLANES, SUBLANES), SUBLANES)

        def v_tiles(q, slot):
            v8 = v_ref[rows_of(q), :]
            for g2 in range(2):
                vbuf[slot, g2] = _rows_to_columns(v8[:, quads[g2]], diag2, bd2)

        def chain(q, slot):
            rows8 = rows_of(q)
            a8, w8, b8, k8, r8 = (x[rows8, :] for x in (a_ref, w_ref, b_ref, k_ref, r_ref))
            pairs = [slice(p * LANES, (p + 1) * LANES) for p in range(N_PAIRS)]
            a_next = pltpu.roll(a8, SUBLANES - 1, 0)
            (a8_l, a8_r), (wa8_l, wa8_r) = _halves(a8), _halves(w8 * a_next)
            ba8 =jnp.concatenate([_pair_sum(b8[:, pr] * a_next[:, pr], left[0:SUBLANES]) for pr in pairs], axis=1)
            ka8 = jnp.concatenate([_pair_sum(k8[:, pr] * a_next[:, pr], left[0:SUBLANES]) for pr in pairs], axis=1)
            sp = [state[p] for p in range(N_PAIRS)]
            for i in range(0, SUBLANES, 2):
                r0, r1 = slice(i, i + 1), slice(i + 1, i + 2)
                sums = [(_pair_dot(sp[p], a8_l[r0, pairs[p]], a8_r[r0, pairs[p]], left),
                         _pair_dot(sp[p], wa8_l[r0, pairs[p]], wa8_r[r0, pairs[p]], left)) for p in range(N_PAIRS)]
                sa0, sa1 = [s[0] for s in sums], [s[1] for s in sums]
                for p in range(N_PAIRS):
                    pr = pairs[p]
                    inner = slice((p % 2) * LANES, (p % 2 + 1) * LANES)
                    vt0 = vbuf[slot, p // 2, i * HEAD:(i + 1) * HEAD, inner]
                    vt1 = vbuf[slot, p // 2, (i + 1) * HEAD:(i + 2) * HEAD, inner]
                    sa_next = sa1[p] + sa0[p] * ba8[r0, pr] + vt0 * ka8[r0, pr]
                    s1 = sp[p] * w8[r0, pr] + sa0[p] * b8[r0, pr] + vt0 * k8[r0, pr]
                    st_ref[q * SUBLANES + i, p] = s1
                    _store_tile(qbuf, slot, p, i, s1 * r8[r0, pr])
                    s2 = s1 * w8[r1, pr] + sa_next * b8[r1, pr] + vt1 * k8[r1, pr]
                    st_ref[q * SUBLANES + i + 1, p] = s2
                    _store_tile(qbuf, slot, p, i + 1, s2 * r8[r1, pr])
                    sp[p] = s2
            for p in range(N_PAIRS):
                state[p] = sp[p]

        def y_rows(q, slot):
            for g2 in range(2):
                y_ref[rows_of(q), quads[g2]] = _diag_rows(qbuf[slot, g2], diag2, bd2, sub_row2)

        v_tiles(0, 0)

        def two_groups(j, carry):
            q0 = 2 * j
            v_tiles(q0 + 1, 1)
            chain(q0, 0)
            y_rows(jnp.maximum(q0 - 1, 0), 1)
            v_tiles(jnp.minimum(q0 + 2, groups - 1), 0)
            chain(q0 + 1, 1)
            y_rows(q0, 0)
            return carry

        lax.fori_loop(0, groups // 2, two_groups, 0)
        y_rows(groups - 1, 1)
        if moves:
            moves.wait(also=(g == nb - 1))

    rows = pl.BlockSpec((tb, D_HALF), lambda g: (g, 0))
    ex_in = exchange.operands if exchange else []
    ex_out = exchange.out_shapes if exchange else []
    res = pl.pallas_call(
        body, name="wkv_fwd", grid=(nb,),
        in_specs=[rows] * 6 + [ANY] * len(ex_in),
        out_specs=[rows, pl.BlockSpec((tb, N_PAIRS, HEAD, LANES), lambda g: (g, 0, 0, 0))] + [ANY] * len(ex_out),
        out_shape=[jax.ShapeDtypeStruct((s, D_HALF), F32),
                   jax.ShapeDtypeStruct((s, N_PAIRS, HEAD, LANES), F32)] + ex_out,
        scratch_shapes=[pltpu.VMEM((N_PAIRS, HEAD, LANES), F32),
                        pltpu.VMEM((2, 2, SUBLANES * HEAD, 2 * LANES), F32),
                        pltpu.VMEM((2, 2, SUBLANES * HEAD, 2 * LANES), BF16)]
                       + (exchange.scratch() if exchange else []),
        compiler_params=_params("arbitrary"),
    )(r, w, k, a, b, v, *ex_in)
    return res[0], res[1], list(res[2:])


def _wkv_bwd(r, w, k, a, b, v, dy, st, exchange=None):
    s = r.shape[0]
    tb = SCAN_TB
    nb = s // tb

    def body(*refs):
        ((r_ref, w_ref, k_ref, a_ref, b_ref, v_ref, dy_ref, st_ref, before_ref),
         (dr_ref, dw_ref, dk_ref, dv_ref, da_ref, db_ref), (dstate, vbuf, qbuf, sbuf),
         moves) = _split_refs(refs, 9, 6, exchange)
        g = pl.program_id(0)
        first_block = g == nb - 1
        if moves:
            moves.start(also=(g == 0))

        @pl.when(g == 0)
        def _():
            dstate[...] = jnp.zeros_like(dstate)
            qbuf[...] = jnp.zeros_like(qbuf)

        left = _left_half()
        diag2, bd2 = _quad_consts()
        sub_row = lax.broadcasted_iota(jnp.int32, (SUBLANES, LANES), 0)
        sub_row2 = lax.broadcasted_iota(jnp.int32, (SUBLANES, 2 * LANES), 0)
        groups = tb // SUBLANES
        quads = [slice(g2 * 2 * LANES, (g2 + 1) * 2 * LANES) for g2 in range(2)]
        row_refs = (dr_ref, dw_ref, dk_ref, da_ref, db_ref)

        def rows_of(q):
            return pl.ds(pl.multiple_of(q * SUBLANES, SUBLANES), SUBLANES)

        def state_before(q, i, p):
            if i > 0:
                return st_ref[q * SUBLANES + i - 1, p]
            return jnp.where(q == 0, jnp.where(first_block, 0.0, before_ref[0, p]),
                             st_ref[jnp.maximum(q * SUBLANES - 1, 0), p])

        def column_tiles(q, slot):
            rows8 = rows_of(q)
            for kind, ref in enumerate((v_ref, dy_ref)):
                x8 = ref[rows8, :]
                for g2 in range(2):
                    vbuf[slot, kind, g2] = _rows_to_columns(x8[:, quads[g2]], diag2, bd2)
            a8 = a_ref[rows8, :]
            for i in range(SUBLANES):
                for p in range(N_PAIRS):
                    _store_tile(sbuf, 0, p, i, state_before(q, i, p) * a8[i:i + 1, p * LANES:(p + 1) * LANES])
            for g2 in range(2):
                vbuf[slot, 2, g2] = jnp.dot(sbuf[0, g2], bd2, preferred_element_type=F32)

        def chain(q, slot):
            rows8 = rows_of(q)
            a8, w8, b8, k8, r8 = (x[rows8, :] for x in (a_ref, w_ref, b_ref, k_ref, r_ref))
            b8_l, b8_r = _halves(b8)
            dsp = [dstate[p] for p in range(N_PAIRS)]
            outs = [[jnp.zeros((SUBLANES, LANES), F32) for _ in row_refs] for _ in range(N_PAIRS)]
            after = [st_ref[q * SUBLANES + SUBLANES - 1, p] for p in range(N_PAIRS)]
            for i in reversed(range(SUBLANES)):
                row = slice(i, i + 1)
                pl_ = [slice(p * LANES, (p + 1) * LANES) for p in range(N_PAIRS)]
                tile = [(p // 2, slice(i * HEAD, (i + 1) * HEAD), slice((p % 2) * LANES, (p % 2 + 1) * LANES))
                        for p in range(N_PAIRS)]
                sp = [state_before(q, i, p) for p in range(N_PAIRS)]
                dyt = [vbuf[(slot, 1) + tile[p]] for p in range(N_PAIRS)]
                ds = [dsp[p] + dyt[p] * r8[row, pl_[p]] for p in range(N_PAIRS)]
                dsa = [_pair_dot(ds[p], b8_l[row, pl_[p]], b8_r[row, pl_[p]], left) for p in range(N_PAIRS)]
                sa = [vbuf[(slot, 2) + tile[p]] for p in range(N_PAIRS)]
                for p in range(N_PAIRS):
                    ar, wr, br, kr = (x[row, pl_[p]] for x in (a8, w8, b8, k8))
                    vt = vbuf[(slot, 0) + tile[p]]
                    dsp[p] = ds[p] * wr + dsa[p] * ar
                    new = (_colsum(after[p] * dyt[p]), _colsum(ds[p] * sp[p]), _colsum(ds[p] * vt),
                           _colsum(sp[p] * dsa[p]), _colsum(ds[p] * sa[p]))
                    outs[p] = [jnp.where(sub_row == i, n, o) for n, o in zip(new, outs[p])]
                    _store_tile(qbuf, slot, p, i, ds[p] * kr)
                after = sp
            for p in range(N_PAIRS):
                dstate[p] = dsp[p]
                for ref, o in zip(row_refs, outs[p]):
                    ref[rows8, p * LANES:(p + 1) * LANES] = o

        def dv_rows(q, slot):
            for g2 in range(2):
                dv_ref[rows_of(q), quads[g2]] = _diag_rows(qbuf[slot, g2], diag2, bd2, sub_row2)

        column_tiles(groups - 1, 0)

        def two_groups(j, carry):
            q0 = groups - 1 - 2 * j
            column_tiles(q0 - 1, 1)
            chain(q0, 0)
            dv_rows(jnp.minimum(q0 + 1, groups - 1), 1)
            column_tiles(jnp.maximum(q0 - 2, 0), 0)
            chain(q0 - 1, 1)
            dv_rows(q0, 0)
            return carry

        lax.fori_loop(0, groups // 2, two_groups, 0)
        dv_rows(0, 1)
        if moves:
            moves.wait(also=(g == nb - 1))

    rows = pl.BlockSpec((tb, D_HALF), lambda g: (nb - 1 - g, 0))
    ex_in = exchange.operands if exchange else []
    ex_out = exchange.out_shapes if exchange else []
    res = pl.pallas_call(
        body, name="wkv_bwd", grid=(nb,),
        in_specs=[rows] * 7 + [pl.BlockSpec((tb, N_PAIRS, HEAD, LANES), lambda g: (nb - 1 - g, 0, 0, 0)),
                               pl.BlockSpec((1, N_PAIRS, HEAD, LANES),
                                            lambda g: (jnp.maximum((nb - 1 - g) * tb - 1, 0), 0, 0, 0))]
                 + [ANY] * len(ex_in),
        out_specs=[rows] * 6 + [ANY] * len(ex_out),
        out_shape=[jax.ShapeDtypeStruct((s, D_HALF), F32)] * 6 + ex_out,
        scratch_shapes=[pltpu.VMEM((N_PAIRS, HEAD, LANES), F32),
                        pltpu.VMEM((2, 3, 2, SUBLANES * HEAD, 2 * LANES), F32),
                        pltpu.VMEM((2, 2, SUBLANES * HEAD, 2 * LANES), BF16),
                        pltpu.VMEM((1, 2, SUBLANES * HEAD, 2 * LANES), BF16)]
                       + (exchange.scratch() if exchange else []),
        compiler_params=_params("arbitrary"),
    )(r, w, k, a, b, v, dy, st, st, *ex_in)
    return list(res[:6]), list(res[6:])


def _rwkv_post_math(y, r, k, v, gate, lw, lb, rk, bd):
    mean = _head_sum(y, bd) * (1.0 / HEAD)
    yc = y - mean
    var = _head_sum(yc * yc, bd) * (1.0 / HEAD)
    rstd = lax.rsqrt(var + LNX_EPS)
    yn = yc * rstd
    rkk = _head_sum(r * k * rk, bd)
    sg = _sigmoid(gate)
    pre = yn * lw + lb + rkk * v
    return yn, rstd, rkk, sg, pre


def _rwkv_prep_bwd(u_a, h_t, grads, mu, wl, w0, a0, kkw, kaw, tm=256):
    s = u_a.shape[0]
    nb = s // tm
    d = h_t.shape[0]

    def body(ua_ref, prev_ref, ht_ref, drs_ref, dws_ref, dks_ref, dvs_ref, das_ref, dbs_ref, drb_ref, dkb_ref, dvb_ref,
             dgt_ref, mu_ref, wl_ref, w0_ref, a0_ref, kkw_ref, kaw_ref,
             du_ref, dwa_ref, dmu_ref, dwl_ref, dw0_ref, da0_ref, dkkw_ref, dkaw_ref, carry):
        i = pl.program_id(0)

        @pl.when(i == 0)
        def _():
            carry[...] = jnp.zeros_like(carry)
            for ref in (dwa_ref, dmu_ref, dwl_ref, dw0_ref, da0_ref, dkkw_ref, dkaw_ref):
                ref[...] = jnp.zeros_like(ref)

        bd = _head_ones()
        mu_v, wl_v, kkw_v, kaw_v = mu_ref[...], wl_ref[...], kkw_ref[...], kaw_ref[...]
        f = _rwkv_elementwise(ua_ref[...], prev_ref[7:8, :], i == nb - 1, mu_v, wl_v, w0_ref[...],
                              a0_ref[...], kkw_v, kaw_v, bd)
        a, kk, k0 = f["a"], f["kk"], f["k0"]
        dk = dks_ref[...] + dkb_ref[...]
        dbs = dbs_ref[...]
        dkk = dbs * a - das_ref[...]
        da = dbs * kk + dk * k0 * kaw_v
        dk0 = dk * (1.0 + (a - 1.0) * kaw_v)
        dkaw_ref[...] += _colsum(dk * k0 * (a - 1.0))
        inv = 1.0 / f["nrm"]
        proj = _head_sum(dkk * kk, bd)
        dkk0 = jnp.where(f["ss"] > 1e-24, (dkk - kk * proj) * inv, dkk * inv)
        dk0 = dk0 + dkk0 * kkw_v
        dkkw_ref[...] += _colsum(dkk0 * k0)
        dza = da * a * (1.0 - a)
        da0_ref[...] += _colsum(dza)
        dz = -dws_ref[...] * f["dec"] * f["e"] * (1.0 - f["sz"])
        dw0_ref[...] += _colsum(dz)
        dll = jnp.concatenate([dz, dza], axis=1).astype(BF16)
        dwl_ref[...] += _dot_tn(f["lin"].astype(BF16), dll)
        dlin = _dot_nt(dll, wl_v)
        lane = lax.broadcasted_iota(jnp.int32, (1, LANES), 1)
        th = f["th"]
        dlo = jnp.where(lane < LORA, dlin * (1.0 - th * th), dlin)
        dus = jnp.concatenate([drs_ref[...] + drb_ref[...], dk0, dvs_ref[...] + dvb_ref[...], dlo, dgt_ref[...]],
                              axis=1)
        dmu_ref[...] += _colsum(dus * f["delta"])
        g1 = dus * mu_v
        rows = lax.broadcasted_iota(jnp.int32, (tm, 1), 0)
        up = jnp.where(rows == tm - 1, carry[...], pltpu.roll(g1, tm - 1, 0))
        dua = dus - g1 + up
        du_ref[...] = dua
        dwa_ref[...] += jnp.dot(ht_ref[...], dua.astype(BF16), preferred_element_type=F32)
        carry[...] = g1[0:1, :]

    rev = lambda w: pl.BlockSpec((tm, w), lambda i: (nb - 1 - i, 0))
    vec = lambda w: pl.BlockSpec((1, w), lambda i: (0, 0))
    wl_spec = pl.BlockSpec((LANES, 2 * D_HALF), lambda i: (0, 0))
    return pl.pallas_call(
        body, name="rwkv_prep_bwd", grid=(nb,),
        in_specs=[rev(SEC), pl.BlockSpec((8, SEC), lambda i: (jnp.maximum((nb - 1 - i) * (tm // 8) - 1, 0), 0)),
                  pl.BlockSpec((d, tm), lambda i: (0, nb - 1 - i))]
                 + [rev(D_HALF)] * 10 + [vec(SEC), wl_spec] + [vec(D_HALF)] * 4,
        out_specs=[rev(SEC), pl.BlockSpec((d, SEC), lambda i: (0, 0)), vec(SEC), wl_spec] + [vec(D_HALF)] * 4,
        out_shape=[jax.ShapeDtypeStruct((s, SEC), F32), jax.ShapeDtypeStruct((d, SEC), F32),
                   jax.ShapeDtypeStruct((1, SEC), F32),
                   jax.ShapeDtypeStruct((LANES, 2 * D_HALF), F32)] + [jax.ShapeDtypeStruct((1, D_HALF), F32)] * 4,
        scratch_shapes=[pltpu.VMEM((1, SEC), F32)],
        compiler_params=_params("arbitrary"),
    )(u_a, u_a, h_t, *grads, mu, wl, w0, a0, kkw, kaw)


def _tri(tm, lower):
    r = lax.broadcasted_iota(jnp.int32, (tm, tm), 0)
    c = lax.broadcasted_iota(jnp.int32, (tm, tm), 1)
    return ((r >= c) if lower else (r <= c)).astype(BF16)


def _head_rms(x, g, bd):
    rinv = lax.rsqrt(_head_sum(x * x, bd) * (1.0 / HEAD) + RMS_EPS)
    xh = x * rinv
    return xh, rinv, xh * g


def _fox_front(h, w_b, fb, qg, kg, tm=256):
    s, d = h.shape

    def body(h_ref, wb_ref, fb_ref, qg_ref, kg_ref, ub_ref, q_ref, k_ref, v_ref, cc_ref, cr_ref, carry):
        i = pl.program_id(0)

        @pl.when(i == 0)
        def _():
            carry[...] = jnp.zeros_like(carry)

        ub_ref[...] = jnp.dot(h_ref[...], wb_ref[...], preferred_element_type=F32)
        bd = _head_ones()
        _, _, qn = _head_rms(ub_ref[:, 0:512], qg_ref[...], bd)
        _, _, kn = _head_rms(ub_ref[:, 512:1024], kg_ref[...], bd)
        q_ref[...] = (qn * ATT_SCALE).astype(BF16)
        k_ref[...] = kn.astype(BF16)
        v_ref[...] = ub_ref[:, 1024:1536].astype(BF16)
        lane = lax.broadcasted_iota(jnp.int32, (1, LANES), 1)
        logf = jnp.where(lane < N_HEADS, _log_sigmoid(ub_ref[:, 2048:2176] + fb_ref[...]), 0.0)
        cum = _exact_dot(logf, _tri(tm, True), ones_first=True) + carry[...]
        for h in range(N_HEADS):
            cc_ref[h] = jnp.broadcast_to(cum[:, h:h + 1], (tm, LANES))
        cr_ref[...] = jnp.transpose(cum)[0:N_HEADS, :]
        carry[...] = cum[tm - 1:tm, :]

    blk = pl.BlockSpec((tm, D_HALF), lambda i: (i, 0))
    return pl.pallas_call(
        body, name="fox_front", grid=(s // tm,),
        in_specs=[pl.BlockSpec((tm, d), lambda i: (i, 0)),
                  pl.BlockSpec(w_b.shape, lambda i: (0, 0), pipeline_mode=pl.Buffered(1)),
                  pl.BlockSpec((1, LANES), lambda i: (0, 0)),
                  pl.BlockSpec((1, D_HALF), lambda i: (0, 0)), pl.BlockSpec((1, D_HALF), lambda i: (0, 0))],
        out_specs=[pl.BlockSpec((tm, SEC), lambda i: (i, 0)), blk, blk, blk,
                   pl.BlockSpec((N_HEADS, tm, LANES), lambda i: (0, i, 0)), pl.BlockSpec((N_HEADS, tm), lambda i: (0, i))],
        out_shape=[jax.ShapeDtypeStruct((s, SEC), F32)] + [jax.ShapeDtypeStruct((s, D_HALF), BF16)] * 3
                  + [jax.ShapeDtypeStruct((N_HEADS, s, LANES), F32), jax.ShapeDtypeStruct((N_HEADS, s), F32)],
        scratch_shapes=[pltpu.VMEM((1, LANES), F32)],
        compiler_params=_params("arbitrary"),
    )(h, w_b, fb, qg, kg)


ATT_T = 256


def _tiles(nblk, by_query):
    if by_query:
        pairs = [(i, j) for i in range(nblk) for j in range(i + 1)]
    else:
        pairs = [(i, j) for j in range(nblk) for i in range(j, nblk)]
    return (jnp.asarray([p[0] for p in pairs], jnp.int32), jnp.asarray([p[1] for p in pairs], jnp.int32))


def _attn_fwd(q, k, v, cc, cr):
    s = q.shape[0]
    t = ATT_T
    nblk = s // t

    def body(qi_ref, kj_ref, q_ref, k_ref, v_ref, cc_ref, cr_ref, o_ref, lse_ref, m_sc, l_sc, acc_sc):
        i = qi_ref[pl.program_id(0)]
        j = kj_ref[pl.program_id(0)]

        @pl.when(j == 0)
        def _():
            m_sc[...] = jnp.full_like(m_sc, NEG)
            l_sc[...] = jnp.zeros_like(l_sc)
            acc_sc[...] = jnp.zeros_like(acc_sc)

        def tile(on_diagonal):
            causal = _causal_tile(t) if on_diagonal else None
            left = lax.broadcasted_iota(jnp.int32, (1, LANES), 1) < HEAD
            for p in range(N_PAIRS):
                lanes = slice(p * LANES, (p + 1) * LANES)
                q2, k2, v2 = q_ref[:, lanes], k_ref[:, lanes], v_ref[:, lanes]
                acc2 = acc_sc[:, lanes]
                for e in range(2):
                    h = 2 * p + e
                    msk = left if e == 0 else jnp.logical_not(left)
                    sc = _dot_nt(jnp.where(msk, q2, jnp.zeros_like(q2)), k2)
                    sc = sc + (_wide(cc_ref[h]) - cr_ref[h:h + 1, :])
                    if on_diagonal:
                        sc = jnp.where(causal, sc, NEG)
                    m_prev = m_sc[h]
                    m_new = jnp.maximum(m_prev, jnp.max(sc, axis=1, keepdims=True))
                    alpha = jnp.exp(m_prev - m_new)
                    pm = jnp.exp(sc - _wide(m_new))
                    l_sc[h] = alpha * l_sc[h] + jnp.sum(pm, axis=1, keepdims=True)
                    m_sc[h] = m_new
                    pv = jnp.dot(pm.astype(BF16), v2, preferred_element_type=F32)
                    acc2 = jnp.where(msk, alpha * acc2 + pv, acc2)
                acc_sc[:, lanes] = acc2

        pl.when(j < i)(functools.partial(tile, False))
        pl.when(j == i)(functools.partial(tile, True))

        @pl.when(j == i)
        def _():
            left = lax.broadcasted_iota(jnp.int32, (1, LANES), 1) < HEAD
            for p in range(N_PAIRS):
                lanes = slice(p * LANES, (p + 1) * LANES)
                inv = jnp.where(left, 1.0 / l_sc[2 * p], 1.0 / l_sc[2 * p + 1])
                o_ref[:, lanes] = acc_sc[:, lanes] * inv
            for h in range(N_HEADS):
                lse_ref[h] = m_sc[h] + jnp.log(l_sc[h])

    qi, kj = _tiles(nblk, by_query=True)
    qblk = pl.BlockSpec((t, D_HALF), lambda n, qi, kj: (qi[n], 0))
    kblk = pl.BlockSpec((t, D_HALF), lambda n, qi, kj: (kj[n], 0))
    qrep = pl.BlockSpec((N_HEADS, t, LANES), lambda n, qi, kj: (0, qi[n], 0))
    return pl.pallas_call(
        body, name="fox_attn_fwd",
        grid_spec=pltpu.PrefetchScalarGridSpec(
            num_scalar_prefetch=2, grid=(qi.shape[0],),
            in_specs=[qblk, kblk, kblk, qrep, pl.BlockSpec((N_HEADS, t), lambda n, qi, kj: (0, kj[n]))],
            out_specs=[qblk, qrep],
            scratch_shapes=[pltpu.VMEM((N_HEADS, t, LANES), F32), pltpu.VMEM((N_HEADS, t, LANES), F32),
                            pltpu.VMEM((t, D_HALF), F32)]),
        out_shape=[jax.ShapeDtypeStruct((s, D_HALF), F32), jax.ShapeDtypeStruct((N_HEADS, s, LANES), F32)],
        compiler_params=_params("arbitrary"),
    )(qi, kj, q, k, v, cc, cr)


def _causal_tile(t):
    return lax.broadcasted_iota(jnp.int32, (t, t), 0) >= lax.broadcasted_iota(jnp.int32, (t, t), 1)


def _wide(x):
    return jnp.concatenate([x, x], axis=1)


def _attn_probs(q2, k2, v2, do2, msk, causal, bias, lse_rows):
    zero = jnp.zeros_like(q2)
    qh = jnp.where(msk, q2, zero)
    doh = jnp.where(msk, do2, zero)
    sc = _dot_nt(qh, k2) + bias
    if causal is not None:
        sc = jnp.where(causal, sc, NEG)
    pm = jnp.exp(sc - _wide(lse_rows))
    dp = _dot_nt(doh, v2)
    return qh, doh, pm, dp


def _attn_bwd_rowdot(q, k, v, do, lse, cc, cr):
    s = q.shape[0]
    t = ATT_T
    nblk = s // t

    def body(qi_ref, kj_ref, q_ref, k_ref, v_ref, do_ref, lse_ref, cc_ref, cr_ref, dd_ref, acc):
        i = qi_ref[pl.program_id(0)]
        j = kj_ref[pl.program_id(0)]

        @pl.when(j == 0)
        def _():
            acc[...] = jnp.zeros_like(acc)

        def tile(on_diagonal):
            causal = _causal_tile(t) if on_diagonal else None
            left = lax.broadcasted_iota(jnp.int32, (1, LANES), 1) < HEAD
            for p in range(N_PAIRS):
                lanes = slice(p * LANES, (p + 1) * LANES)
                q2, k2, v2, do2 = q_ref[:, lanes], k_ref[:, lanes], v_ref[:, lanes], do_ref[:, lanes]
                for e in range(2):
                    h = 2 * p + e
                    msk = left if e == 0 else jnp.logical_not(left)
                    bias = _wide(cc_ref[h]) - cr_ref[h:h + 1, :]
                    _, _, pm, dp = _attn_probs(q2, k2, v2, do2, msk, causal, bias, lse_ref[h])
                    acc[h] += jnp.sum(pm * dp, axis=1, keepdims=True)

        pl.when(j < i)(functools.partial(tile, False))
        pl.when(j == i)(functools.partial(tile, True))

        @pl.when(j == i)
        def _():
            dd_ref[...] = acc[...]

    qi, kj = _tiles(nblk, by_query=True)
    qblk = pl.BlockSpec((t, D_HALF), lambda n, qi, kj: (qi[n], 0))
    qcol = pl.BlockSpec((N_HEADS, t, LANES), lambda n, qi, kj: (0, qi[n], 0))
    kblk = pl.BlockSpec((t, D_HALF), lambda n, qi, kj: (kj[n], 0))
    return pl.pallas_call(
        body, name="fox_attn_rowdot",
        grid_spec=pltpu.PrefetchScalarGridSpec(
            num_scalar_prefetch=2, grid=(qi.shape[0],),
            in_specs=[qblk, kblk, kblk, qblk, qcol, qcol, pl.BlockSpec((N_HEADS, t), lambda n, qi, kj: (0, kj[n]))],
            out_specs=qcol, scratch_shapes=[pltpu.VMEM((N_HEADS, t, LANES), F32)]),
        out_shape=jax.ShapeDtypeStruct((N_HEADS, s, LANES), F32),
        compiler_params=_params("arbitrary"),
    )(qi, kj, q, k, v, do, lse, cc, cr)


def _attn_bwd(q, k, v, do, lse, dd, cc, cr):
    s = q.shape[0]
    t = ATT_T
    nblk = s // t

    def body(qi_ref, kj_ref, q_ref, k_ref, v_ref, do_ref, lse_ref, dd_ref, cc_ref, cr_ref,
             dq_ref, dk_ref, dv_ref, dcr_ref, dk_sc, dv_sc, dcr_sc):
        i = qi_ref[pl.program_id(0)]
        j = kj_ref[pl.program_id(0)]

        @pl.when(pl.program_id(0) == 0)
        def _():
            dq_ref[...] = jnp.zeros_like(dq_ref)

        @pl.when(i == j)
        def _():
            dk_sc[...] = jnp.zeros_like(dk_sc)
            dv_sc[...] = jnp.zeros_like(dv_sc)
            dcr_sc[...] = jnp.zeros_like(dcr_sc)

        def tile(on_diagonal):
            causal = _causal_tile(t) if on_diagonal else None
            left = lax.broadcasted_iota(jnp.int32, (1, LANES), 1) < HEAD
            qrows = pl.ds(pl.multiple_of(i * t, t), t)
            for p in range(N_PAIRS):
                lanes = slice(p * LANES, (p + 1) * LANES)
                q2, k2, v2, do2 = q_ref[:, lanes], k_ref[:, lanes], v_ref[:, lanes], do_ref[:, lanes]
                zero = jnp.zeros_like(q2)
                dq2 = jnp.zeros((t, LANES), F32)
                dk2 = jnp.zeros((t, LANES), F32)
                dv2 = jnp.zeros((t, LANES), F32)
                for e in range(2):
                    h = 2 * p + e
                    msk = left if e == 0 else jnp.logical_not(left)
                    bias = _wide(cc_ref[h]) - cr_ref[h:h + 1, :]
                    qh, doh, pm, dp = _attn_probs(q2, k2, v2, do2, msk, causal, bias, lse_ref[h])
                    dsc = pm * (dp - _wide(dd_ref[h]))
                    dsb = dsc.astype(BF16)
                    dv2 += _dot_tn(pm.astype(BF16), doh)
                    dk2 += _dot_tn(dsb, qh)
                    dq2 += jnp.dot(dsb, jnp.where(msk, k2, zero), preferred_element_type=F32)
                    dcr_sc[h:h + 1, :] += -_colsum(dsc)
                dq_ref[qrows, lanes] += dq2 * ATT_SCALE
                dk_sc[:, lanes] += dk2
                dv_sc[:, lanes] += dv2

        pl.when(i > j)(functools.partial(tile, False))
        pl.when(i == j)(functools.partial(tile, True))

        @pl.when(i == nblk - 1)
        def _():
            dk_ref[...] = dk_sc[...]
            dv_ref[...] = dv_sc[...]
            dcr_ref[...] = dcr_sc[...]

    qi, kj = _tiles(nblk, by_query=False)
    qblk = pl.BlockSpec((t, D_HALF), lambda n, qi, kj: (qi[n], 0))
    qcol = pl.BlockSpec((N_HEADS, t, LANES), lambda n, qi, kj: (0, qi[n], 0))
    kblk = pl.BlockSpec((t, D_HALF), lambda n, qi, kj: (kj[n], 0))
    krow = pl.BlockSpec((N_HEADS, t), lambda n, qi, kj: (0, kj[n]))
    return pl.pallas_call(
        body, name="fox_attn_bwd",
        grid_spec=pltpu.PrefetchScalarGridSpec(
            num_scalar_prefetch=2, grid=(qi.shape[0],),
            in_specs=[qblk, kblk, kblk, qblk, qcol, qcol, qcol, krow],
            out_specs=[pl.BlockSpec((s, D_HALF), lambda n, qi, kj: (0, 0)), kblk, kblk, krow],
            scratch_shapes=[pltpu.VMEM((t, D_HALF), F32), pltpu.VMEM((t, D_HALF), F32), pltpu.VMEM((N_HEADS, t), F32)]),
        out_shape=[jax.ShapeDtypeStruct((s, D_HALF), F32)] * 3 + [jax.ShapeDtypeStruct((N_HEADS, s), F32)],
        compiler_params=_params("arbitrary"),
    )(qi, kj, q, k, v, do, lse, dd, cc, cr)


def _fox_prep_bwd(u_b, h_t, dq, dk, dv, dgate, dcum, fb, qg, kg, tm=256):
    s = u_b.shape[0]
    nb = s // tm
    d = h_t.shape[0]

    def body(ub_ref, ht_ref, dq_ref, dk_ref, dv_ref, dg_ref, dc_ref, fb_ref, qg_ref, kg_ref,
             du_ref, dwb_ref, dqg_ref, dkg_ref, dfb_ref, carry):
        i = pl.program_id(0)

        @pl.when(i == 0)
        def _():
            carry[...] = jnp.zeros_like(carry)
            dwb_ref[...] = jnp.zeros_like(dwb_ref)
            dqg_ref[...] = jnp.zeros_like(dqg_ref)
            dkg_ref[...] = jnp.zeros_like(dkg_ref)
            dfb_ref[...] = jnp.zeros_like(dfb_ref)

        bd = _head_ones()
        for lo, g_ref, d_ref, dgain_ref in ((0, qg_ref, dq_ref, dqg_ref), (512, kg_ref, dk_ref, dkg_ref)):
            gain = g_ref[...]
            xh, rinv, _ = _head_rms(ub_ref[:, lo:lo + 512], gain, bd)
            dn = d_ref[...]
            dgain_ref[...] += _colsum(dn * xh)
            dxh = dn * gain
            du_ref[:, lo:lo + 512] = rinv * (dxh - xh * (_head_sum(dxh * xh, bd) * (1.0 / HEAD)))
        du_ref[:, 1024:1536] = dv_ref[...]
        du_ref[:, 1536:2048] = dg_ref[...]
        lane = lax.broadcasted_iota(jnp.int32, (1, LANES), 1)
        dc = dc_ref[...]
        dlogf = _exact_dot(dc, _tri(tm, False), ones_first=True) + carry[...]
        carry[...] += _colsum(dc)
        fl = ub_ref[:, 2048:2176] + fb_ref[...]
        dfl = jnp.where(lane < N_HEADS, dlogf * (1.0 - _sigmoid(fl)), 0.0)
        du_ref[:, 2048:2176] = dfl
        dfb_ref[...] += _colsum(dfl)
        dwb_ref[...] += jnp.dot(ht_ref[...], du_ref[...].astype(BF16), preferred_element_type=F32)

    rev = lambda w: pl.BlockSpec((tm, w), lambda i: (nb - 1 - i, 0))
    vec = lambda w: pl.BlockSpec((1, w), lambda i: (0, 0))
    return pl.pallas_call(
        body, name="fox_prep_bwd", grid=(nb,),
        in_specs=[rev(SEC), pl.BlockSpec((d, tm), lambda i: (0, nb - 1 - i))] + [rev(D_HALF)] * 4
                 + [rev(LANES), vec(LANES), vec(D_HALF), vec(D_HALF)],
        out_specs=[rev(SEC), pl.BlockSpec((d, SEC), lambda i: (0, 0)), vec(D_HALF), vec(D_HALF), vec(LANES)],
        out_shape=[jax.ShapeDtypeStruct((s, SEC), F32), jax.ShapeDtypeStruct((d, SEC), F32),
                   jax.ShapeDtypeStruct((1, D_HALF), F32), jax.ShapeDtypeStruct((1, D_HALF), F32),
                   jax.ShapeDtypeStruct((1, LANES), F32)],
        scratch_shapes=[pltpu.VMEM((1, LANES), F32)],
        compiler_params=_params("arbitrary"),
    )(u_b, h_t, dq, dk, dv, dgate, dcum, fb, qg, kg)


def _merge(y, r, k, v, gate_a, o, u_b, h, x, tgt, w_g, wa, wb, wo, fg, lw, lb, rk, tm=256):
    s, d = x.shape

    def body(y_ref, r_ref, k_ref, v_ref, ga_ref, o_ref, gb_ref, h_ref, x_ref, t_ref, wg_ref, wa_ref, wb_ref, wo_ref,
             fg_ref, lw_ref, lb_ref, rk_ref,
             dx2_ref, dy_ref, drb_ref, dkb_ref, dvb_ref, dga_ref, do_ref, dgb_ref, dug_ref,
             dwa_ref, dwb_ref, dwo_ref, dfg_ref, loss_ref, dlw_ref, dlb_ref, drk_ref):
        i = pl.program_id(0)

        @pl.when(i == 0)
        def _():
            for ref in (dwa_ref, dwb_ref, dwo_ref, dfg_ref, loss_ref, dlw_ref, dlb_ref, drk_ref):
                ref[...] = jnp.zeros_like(ref)

        bd = _head_ones()
        wa_v, wb_v, wo_v, fg_v = wa_ref[...], wb_ref[...], wo_ref[...], fg_ref[...]
        rv, kv, vv, ga, lw_v, rk_v = r_ref[...], k_ref[...], v_ref[...], ga_ref[...], lw_ref[...], rk_ref[...]
        yn, rstd, rkk, sga, pre = _rwkv_post_math(y_ref[...], rv, kv, vv, ga, lw_v, lb_ref[...], rk_v, bd)
        silu_a = ga * sga
        gb, ov = gb_ref[...], o_ref[...]
        sgb = _sigmoid(gb)
        silu_b = gb * sgb
        ma = (pre * silu_a).astype(BF16)
        mb = (ov * silu_b).astype(BF16)
        ya = jnp.dot(ma, wa_v, preferred_element_type=F32)
        yb = jnp.dot(mb, wb_v, preferred_element_type=F32)
        ug = jnp.dot(h_ref[...], wg_ref[...], preferred_element_type=F32)
        sa = _sigmoid(ug[:, 0:d])
        sb = _sigmoid(ug[:, d:2 * d])
        merged = (sa * ya + sb * yb).astype(BF16)
        x2 = x_ref[...] + jnp.dot(merged, wo_v, preferred_element_type=F32)
        r2 = lax.rsqrt(jnp.mean(x2 * x2, axis=-1, keepdims=True) + RMS_EPS)
        x2h = x2 * r2
        err = x2h * fg_v - t_ref[...]
        loss_ref[...] += _colsum(err * err)
        dyo = err * (1.0 / d)
        dfg_ref[...] += _colsum(dyo * x2h)
        dx2h = dyo * fg_v
        dx2 = r2 * (dx2h - x2h * jnp.mean(dx2h * x2h, axis=-1, keepdims=True))
        dx2_ref[...] = dx2
        dx2b = dx2.astype(BF16)
        dmerged = _dot_nt(dx2b, wo_v)
        dwo_ref[...] += _dot_tn(merged, dx2b)
        dya = dmerged * sa
        dyb = dmerged * sb
        dug_ref[:, 0:d] = dya * ya * (1.0 - sa)
        dug_ref[:, d:2 * d] = dyb * yb * (1.0 - sb)
        dyab = dya.astype(BF16)
        dybb = dyb.astype(BF16)
        dwa_ref[...] += _dot_tn(ma, dyab)
        dwb_ref[...] += _dot_tn(mb, dybb)
        dmb = _dot_nt(dybb, wb_v)
        do_ref[...] = (dmb * silu_b).astype(BF16)
        dgb_ref[...] = dmb * ov * (sgb * (1.0 + gb * (1.0 - sgb)))
        dma = _dot_nt(dyab, wa_v)
        dga_ref[...] = dma * pre * (sga * (1.0 + ga * (1.0 - sga)))
        dpre = dma * silu_a
        dlw_ref[...] += _colsum(dpre * yn)
        dlb_ref[...] += _colsum(dpre)
        dyn = dpre * lw_v
        m1 = _head_sum(dyn, bd) * (1.0 / HEAD)
        m2 = _head_sum(dyn * yn, bd) * (1.0 / HEAD)
        dy_ref[...] = rstd * (dyn - m1 - yn * m2)
        dvb_ref[...] = dpre * rkk
        drkk = _head_sum(dpre * vv, bd)
        drb_ref[...] = drkk * kv * rk_v
        dkb_ref[...] = drkk * rv * rk_v
        drk_ref[...] += _colsum(drkk * rv * kv)

    row = lambda w: pl.BlockSpec((tm, w), lambda i: (i, 0))
    full = lambda a: pl.BlockSpec(a.shape, lambda i: (0, 0))
    once = lambda a: pl.BlockSpec(a.shape, lambda i: (0, 0), pipeline_mode=pl.Buffered(1))
    half = jax.ShapeDtypeStruct((s, D_HALF), F32)
    fshape = lambda a: jax.ShapeDtypeStruct(a.shape, F32)
    return pl.pallas_call(
        body, name="merge_fwd_bwd", grid=(s // tm,),
        in_specs=[row(D_HALF)] * 6 + [pl.BlockSpec((tm, D_HALF), lambda i: (i, 3)), row(d), row(d), row(d),
                                      once(w_g), once(wa), once(wb), once(wo), full(fg), full(lw), full(lb), full(rk)],
        out_specs=[row(d)] + [row(D_HALF)] * 7 + [row(GATE_COLS), full(wa), full(wb), full(wo), full(fg), full(fg),
                                                   full(lw), full(lb), full(rk)],
        out_shape=[jax.ShapeDtypeStruct((s, d), F32)] + [half] * 5 + [jax.ShapeDtypeStruct((s, D_HALF), BF16), half,
                                                                    jax.ShapeDtypeStruct((s, GATE_COLS), F32),
                                                                    fshape(wa), fshape(wb), fshape(wo), fshape(fg),
                                                                    fshape(fg), fshape(lw), fshape(lb), fshape(rk)],
        compiler_params=_params("arbitrary"),
    )(y, r, k, v, gate_a, o, u_b, h, x, tgt, w_g, wa, wb, wo, fg, lw, lb, rk)


def _lora_weight(w_up, a_up):
    z = jnp.zeros((LORA, D_HALF), w_up.dtype)
    return jnp.concatenate([jnp.concatenate([w_up, z], axis=1), jnp.concatenate([z, a_up], axis=1)], axis=0)


def _device_grads(x, tgt, p, w_a, w_up, a_up, late_weights, fwd_exchange=None, bwd_exchange=None, tail_exchange=None):
    wl = _lora_weight(w_up, a_up)
    rk = p["r_k"].reshape(1, D_HALF)
    fb = jnp.pad(p["f_bias"], ((0, 0), (0, LANES - N_HEADS)))
    qg = jnp.tile(p["q_norm_g"], (1, N_HEADS))
    kg = jnp.tile(p["k_norm_g"], (1, N_HEADS))
    fg = p["final_norm_g"].reshape(1, D_MODEL)
    mixer = (p["shift_mu"], wl, p["w0"], p["a0"], p["k_k"], p["k_a"])

    h, u_a, r, dec, k, v, av, bv, gate_a = _rwkv_front(x, p["norm_g"], w_a, *mixer)
    y, st, arrived = _wkv_fwd(r, dec, k, av, bv, v, fwd_exchange)

    w_b, w_g, w_out_a, w_out_b, w_out = late_weights(arrived)
    u_b, q, kn, vb, cc, cr = _fox_front(h, w_b, fb, qg, kg)
    o, lse = _attn_fwd(q, kn, vb, cc, cr)

    (dx2, dy, dr_b, dk_b, dv_b, dgate_a, do, dgate_b, du_g, dwa, dwb, dwo, dfg, loss_vec, dlw, dlb, drk) = _merge(
        y, r, k, v, gate_a, o, u_b, h, x, tgt, w_g, w_out_a, w_out_b, w_out, fg, p["lnx_w"], p["lnx_b"], rk)

    dd = _attn_bwd_rowdot(q, kn, vb, do, lse, cc, cr)
    dq, dk_att, dv_att, dcr = _attn_bwd(q, kn, vb, do, lse, dd, cc, cr)
    dcum = jnp.pad(dcr.T, ((0, 0), (0, LANES - N_HEADS)))
    h_t = h.T
    du_b, dw_b, dqg, dkg, dfb = _fox_prep_bwd(u_b, h_t, dq, dk_att, dv_att, dgate_b, dcum, fb, qg, kg)
    dw_g = _matmul_tn_acc(h_t, du_g, "dw_gate")

    scan_grads, sent = _wkv_bwd(r, dec, k, av, bv, v, dy, st,
                                bwd_exchange(dw_b, dw_g, dwa, dwb, dwo) if bwd_exchange else None)
    du_a, dw_a, dmu, dwl, dw0, da0, dkkw, dkaw = _rwkv_prep_bwd(
        u_a, h_t, (*scan_grads, dr_b, dk_b, dv_b, dgate_a), *mixer)
    dw_up, da_up = dwl[:LORA, :D_HALF], dwl[LORA:, D_HALF:]
    sent_last = _run_on_sequencer(tail_exchange(dw_a, dw_up, da_up), "scatter_tail", 1) if tail_exchange else []
    grad_x, dnorm_g, _ = _inproj_bwd(du_a, du_b, du_g, w_a, w_b, w_g, x, dx2, p["norm_g"])

    grads = dict(
        norm_g=dnorm_g, w_in=(dw_a, dw_b, dw_g), shift_mu=dmu,
        w_lora_up=dw_up, w0=dw0, a_lora_up=da_up, a0=da0, k_k=dkkw, k_a=dkaw,
        r_k=drk.reshape(1, N_HEADS, HEAD), lnx_w=dlw, lnx_b=dlb, f_bias=dfb[:, :N_HEADS],
        q_norm_g=dqg.reshape(N_HEADS, HEAD).sum(axis=0, keepdims=True),
        k_norm_g=dkg.reshape(N_HEADS, HEAD).sum(axis=0, keepdims=True),
        w_out_a=dwa, w_out_b=dwb, w_out=dwo, final_norm_g=dfg.reshape(D_MODEL))
    return loss_vec, grad_x, grads, sent, sent_last


CHIP_FLIPS = ((1, 0), (0, 1), (1, 1))
ANY = pl.BlockSpec(memory_space=pl.ANY)


def _position():
    return lax.axis_index("x"), lax.axis_index("y"), lax.axis_index("c")


def _flip(v, f):
    return 1 - v if f else v


def _both(a, b):
    if a is None:
        return b
    return a if b is None else jnp.logical_and(a, b)


def _when(cond, fn):
    if cond is None:
        fn()
    else:
        pl.when(cond)(fn)


class _Moves:
    def __init__(self, send_sems, recv_sems, local_sems):
        self.send_sems, self.recv_sems, self.local_sems = send_sems, recv_sems, local_sems
        self.remote, self.local = [], []

    def send(self, src, dst, peer, landing, send_if=None, recv_if=None, first=False):
        k = len(self.remote)
        sems = dict(send_sem=self.send_sems.at[k], recv_sem=self.recv_sems.at[k], device_id=peer, device_id_type=MESH)
        out = pltpu.make_async_remote_copy(src_ref=src, dst_ref=dst, **sems)
        arrival = pltpu.make_async_remote_copy(src_ref=src, dst_ref=landing, **sems)
        self.remote.append((out, arrival, send_if, recv_if, first))

    def copy(self, src, dst, cond=None):
        cp = pltpu.make_async_copy(src, dst, self.local_sems.at[len(self.local)])
        self.local.append((cp, cond))

    def start(self, also=None):
        for cp, cond in self.local:
            _when(_both(also, cond), cp.start)
        for out, _, send_if, _, _ in self.remote:
            _when(_both(also, send_if), out.start)

    def wait_arrivals(self, also=None, first=None):
        for _, arrival, _, recv_if, is_first in self.remote:
            if first is None or first == is_first:
                _when(_both(also, recv_if), arrival.wait_recv)

    def wait_sent(self, also=None):
        for out, _, send_if, _, _ in self.remote:
            _when(_both(also, send_if), out.wait_send)
        for cp, cond in self.local:
            _when(_both(also, cond), cp.wait)

    def wait(self, also=None):
        self.wait_arrivals(also)
        self.wait_sent(also)


class _Exchange:
    def __init__(self, operands, out_shapes, n_remote, n_local, build, relays=None, in_place=(), n_staging=0):
        self.operands, self.out_shapes = list(operands), list(out_shapes)
        self.n_remote, self.n_local, self.build = n_remote, n_local, build
        self.relays, self.in_place = relays, in_place
        self.n_staging = n_staging

    def scratch(self):
        return [pltpu.SemaphoreType.DMA((self.n_remote,)), pltpu.SemaphoreType.DMA((self.n_remote,)),
                pltpu.SemaphoreType.DMA((max(self.n_local, 1),))]

    def moves(self, in_refs, out_refs, sems):
        mv = _Moves(*sems)
        self.build(mv, in_refs, out_refs)
        return mv


def _run_on_sequencer(exchange, name, collective_id):
    ins = [jax.new_ref(a, memory_space=pltpu.MemorySpace.HBM) for a in exchange.operands]
    outs = [ins[i] if i in exchange.in_place else jax.empty_ref(s, memory_space=pltpu.MemorySpace.HBM)
            for i, s in enumerate(exchange.out_shapes)]
    forward, to_sibling = exchange.relays or (None, None)
    relay_scratch = [pltpu.SemaphoreType.DMA((stage[0],)) for stage in (forward, to_sibling) if stage for _ in range(2)]

    def launch(*sems):
        x, y, c = _position()
        peers = [(_flip(x, fx), _flip(y, fy), c) for fx, fy in CHIP_FLIPS] + ([(x, y, 1 - c)] if to_sibling else [])
        barrier = pltpu.get_barrier_semaphore()
        for peer in peers:
            pl.semaphore_signal(barrier, inc=1, device_id=peer, device_id_type=MESH)
        pl.semaphore_wait(barrier, len(peers))
        moves = exchange.moves(ins, outs, sems[:3])
        moves.start()
        later = []
        if forward:
            onward = _Moves(sems[3], sems[4], None)
            forward[1](onward, ins, outs)
            moves.wait_arrivals(first=True)
            onward.start()
            moves.wait_arrivals(first=False)
            onward.wait_arrivals()
            later.append(onward)
        else:
            moves.wait_arrivals()
        if to_sibling:
            passed = _Moves(*sems[-2:], None)
            to_sibling[1](passed, ins, outs)
            passed.start()
            passed.wait_arrivals()
            later.append(passed)
        for mv in later + [moves]:
            mv.wait_sent()

    pl.kernel(launch, mesh=plsc.ScalarSubcoreMesh(axis_name="sequencer", num_cores=1), name=name,
              scratch_types=tuple(exchange.scratch() + relay_scratch),
              compiler_params=pltpu.CompilerParams(collective_id=collective_id))()
    return [o[...] for o in outs[:len(outs) - exchange.n_staging]]


def _row_major_copy(a, name):
    r, c = a.shape
    tr = _row_tile(r)

    def body(a_ref, o_ref):
        o_ref[...] = a_ref[...]

    blk = pl.BlockSpec((tr, c), lambda i: (i, 0))
    return pl.pallas_call(body, name=name, grid=(r // tr,), in_specs=[blk], out_specs=blk,
                          out_shape=jax.ShapeDtypeStruct(a.shape, a.dtype), compiler_params=_params("parallel"))(a)


def _is_chip(x, y, chip):
    return jnp.logical_and(x == chip // 2, y == chip % 2)


def _gather_exchange(from_chip, from_all, split=()):
    n1, n2 = len(from_chip), len(from_all)
    near = CHIP_FLIPS[:2]

    def quarters(t, c, first, count=1):
        n = from_chip[t][1].shape[0] // 4
        return pl.ds((2 * c + first) * n, count * n)

    def build(mv, ins, outs):
        x, y, c = _position()
        me = 2 * x + y
        for t, (chip, _) in enumerate(from_chip):
            if t not in split:
                mv.copy(ins[t], outs[t], cond=_is_chip(x, y, chip))
        for t in range(n2):
            mv.copy(ins[n1 + t], outs[n1 + t].at[me])
        for t in split:
            for first in (True, False):
                for f, (fx, fy) in enumerate(near):
                    px, py = _flip(x, fx), _flip(y, fy)
                    part = quarters(t, c, f if first else 1 - f)
                    mv.send(ins[t].at[part], outs[t].at[part], (px, py, c), landing=outs[t].at[part], first=first,
                            send_if=_is_chip(x, y, from_chip[t][0]), recv_if=_is_chip(px, py, from_chip[t][0]))
        for fx, fy in CHIP_FLIPS:
            px, py = _flip(x, fx), _flip(y, fy)
            peer = (px, py, c)
            for t, (chip, _) in enumerate(from_chip):
                if t not in split:
                    mv.send(ins[t], outs[t], peer, landing=outs[t],
                            send_if=_is_chip(x, y, chip), recv_if=_is_chip(px, py, chip))
            for t in range(n2):
                mv.send(ins[n1 + t], outs[n1 + t].at[me], peer, landing=outs[n1 + t].at[2 * px + py])

    def forward(mv, ins, outs):
        x, y, c = _position()
        for t in split:
            chip = from_chip[t][0]
            for f, (fx, fy) in enumerate(near):
                gx, gy = near[1 - f]
                part = quarters(t, c, f)
                mv.send(outs[t].at[part], outs[t].at[part], (_flip(x, gx), _flip(y, gy), c), landing=outs[t].at[part],
                        send_if=_is_chip(_flip(x, fx), _flip(y, fy), chip), recv_if=_is_chip(1 - x, 1 - y, chip))

    def to_sibling(mv, ins, outs):
        x, y, c = _position()
        for t in split:
            came = jnp.logical_not(_is_chip(x, y, from_chip[t][0]))
            mv.send(outs[t].at[quarters(t, c, 0, 2)], outs[t].at[quarters(t, c, 0, 2)], (x, y, 1 - c),
                    landing=outs[t].at[quarters(t, 1 - c, 0, 2)], send_if=came, recv_if=came)

    arrays = [a for _, a in from_chip] + list(from_all)
    shapes = [jax.ShapeDtypeStruct(a.shape, a.dtype) for _, a in from_chip]
    shapes += [jax.ShapeDtypeStruct((N_CHIPS,) + a.shape, a.dtype) for a in from_all]
    n_remote = len(CHIP_FLIPS) * (n1 - len(split) + n2) + 2 * len(near) * len(split)
    relays = ((len(near) * len(split), forward), (len(split), to_sibling)) if split else None
    return _Exchange(arrays, shapes, n_remote, n1 + n2, build, relays, in_place=split)


def _scatter_exchange(to_chip, to_all, via_neighbours=False):
    n1, n2 = len(to_chip), len(to_all)
    near = CHIP_FLIPS[:2]
    direct = near if via_neighbours else CHIP_FLIPS

    def half(t, g):
        n = to_chip[t][1].shape[0] // 2
        return pl.ds(g * n, n)

    def build(mv, ins, outs):
        x, y, c = _position()
        if via_neighbours:
            for t, (chip, _) in enumerate(to_chip):
                for g, (gx, gy) in enumerate(near):
                    ox, oy = near[1 - g]
                    mv.send(ins[t].at[half(t, g)], outs[n1 + n2 + t], (_flip(x, gx), _flip(y, gy), c),
                            landing=outs[n1 + n2 + t], first=True, send_if=_is_chip(1 - x, 1 - y, chip),
                            recv_if=_is_chip(_flip(x, ox), _flip(y, oy), chip))
        for f, (fx, fy) in enumerate(CHIP_FLIPS):
            px, py = _flip(x, fx), _flip(y, fy)
            peer = (px, py, c)
            if (fx, fy) in direct:
                for t, (chip, _) in enumerate(to_chip):
                    mv.send(ins[t], outs[t].at[f], peer, landing=outs[t].at[f],
                            send_if=_is_chip(px, py, chip), recv_if=_is_chip(x, y, chip))
            for t in range(n2):
                mv.send(ins[n1 + t].at[2 * px + py], outs[n1 + t].at[f], peer, landing=outs[n1 + t].at[f])

    def forward(mv, ins, outs):
        x, y, c = _position()
        for t, (chip, _) in enumerate(to_chip):
            for g in range(len(near)):
                ox, oy = near[1 - g]
                far_slot = outs[t].at[len(near)].at[half(t, g)]
                mv.send(outs[n1 + n2 + t], far_slot, (_flip(x, ox), _flip(y, oy), c), landing=far_slot,
                        send_if=_is_chip(_flip(x, ox), _flip(y, oy), chip), recv_if=_is_chip(x, y, chip))

    arrays = [a for _, a in to_chip] + list(to_all)
    shapes = [jax.ShapeDtypeStruct((len(CHIP_FLIPS),) + a.shape, a.dtype) for _, a in to_chip]
    shapes += [jax.ShapeDtypeStruct((len(CHIP_FLIPS),) + a.shape[1:], a.dtype) for a in to_all]
    if not via_neighbours:
        return _Exchange(arrays, shapes, len(CHIP_FLIPS) * (n1 + n2), 0, build)
    shapes += [jax.ShapeDtypeStruct((a.shape[0] // 2, a.shape[1]), a.dtype) for _, a in to_chip]
    return _Exchange(arrays, shapes, 2 * len(near) * n1 + len(CHIP_FLIPS) * n2, 0, build,
                     relays=((len(near) * n1, forward), None), n_staging=n1)


def _swap_sibling(tensors, name):
    n = len(tensors)

    def body(*refs):
        ins, outs = refs[:n], refs[n:2 * n]
        send_sems, recv_sems = refs[2 * n:]
        x, y, c = _position()
        copies = [pltpu.make_async_remote_copy(
            src_ref=ins[t], dst_ref=outs[t], send_sem=send_sems.at[t], recv_sem=recv_sems.at[t],
            device_id=(x, y, 1 - c), device_id_type=MESH) for t in range(n)]
        for cp in copies:
            cp.start()
        for cp in copies:
            cp.wait_recv()
        for cp in copies:
            cp.wait_send()

    return pl.pallas_call(
        body, name=name, in_specs=[ANY] * n, out_specs=[ANY] * n,
        out_shape=[jax.ShapeDtypeStruct(a.shape, a.dtype) for a in tensors],
        scratch_shapes=[pltpu.SemaphoreType.DMA((n,)), pltpu.SemaphoreType.DMA((n,))],
        compiler_params=pltpu.CompilerParams(has_side_effects=True),
    )(*tensors)


def _allreduce_small(slab):
    stages = 3

    def body(x_ref, o_ref, buf, send_sems, recv_sems):
        x, y, c = _position()
        peers = ((1 - x, y, c), (x, 1 - y, c), (x, y, 1 - c))
        o_ref[...] = x_ref[...]
        for k, peer in enumerate(peers):
            cp = pltpu.make_async_remote_copy(src_ref=o_ref, dst_ref=buf.at[k], send_sem=send_sems.at[k],
                                              recv_sem=recv_sems.at[k], device_id=peer, device_id_type=MESH)
            cp.start()
            cp.wait()
            o_ref[...] = o_ref[...] + buf[k]

    return pl.pallas_call(
        body, name="allreduce_small",
        in_specs=[pl.BlockSpec(memory_space=pltpu.VMEM)], out_specs=pl.BlockSpec(memory_space=pltpu.VMEM),
        out_shape=jax.ShapeDtypeStruct(slab.shape, slab.dtype),
        scratch_shapes=[pltpu.VMEM((stages,) + slab.shape, slab.dtype),
                        pltpu.SemaphoreType.DMA((stages,)), pltpu.SemaphoreType.DMA((stages,))],
        compiler_params=pltpu.CompilerParams(has_side_effects=True),
    )(slab)


def _row_tile(r):
    return min(r, 256)


def _sum4(stack, recv, me):
    _, r, c = stack.shape
    tr = _row_tile(r)

    def body(me_ref, own_ref, recv_ref, o_ref):
        o_ref[...] = (((own_ref[...] + recv_ref[0].astype(F32)) + recv_ref[1].astype(F32))
                      + recv_ref[2].astype(F32))

    return pl.pallas_call(
        body, name="sum_partials",
        grid_spec=pltpu.PrefetchScalarGridSpec(
            num_scalar_prefetch=1, grid=(r // tr,),
            in_specs=[pl.BlockSpec((None, tr, c), lambda i, me_ref: (me_ref[0], i, 0)),
                      pl.BlockSpec((len(CHIP_FLIPS), tr, c), lambda i, me_ref: (0, i, 0))],
            out_specs=pl.BlockSpec((tr, c), lambda i, me_ref: (i, 0))),
        out_shape=jax.ShapeDtypeStruct((r, c), F32), compiler_params=_params("parallel"),
    )(me, stack, recv)


def _sum_block(own, recv):
    r, c = own.shape
    tr = _row_tile(r)

    def body(own_ref, recv_ref, o_ref):
        o_ref[...] = (((own_ref[...] + recv_ref[0].astype(F32)) + recv_ref[1].astype(F32))
                      + recv_ref[2].astype(F32))

    return pl.pallas_call(
        body, name="sum_block", grid=(r // tr,),
        in_specs=[pl.BlockSpec((tr, c), lambda i: (i, 0)), pl.BlockSpec((len(CHIP_FLIPS), tr, c), lambda i: (0, i, 0))],
        out_specs=pl.BlockSpec((tr, c), lambda i: (i, 0)),
        out_shape=jax.ShapeDtypeStruct((r, c), F32), compiler_params=_params("parallel"),
    )(own, recv)


def _adamw_math(w, g, m, v):
    m = ADAM_B1 * m + (1.0 - ADAM_B1) * g
    v = ADAM_B2 * v + (1.0 - ADAM_B2) * (g * g)
    m_hat = m / (1.0 - ADAM_B1 ** ADAM_STEP)
    v_hat = v / (1.0 - ADAM_B2 ** ADAM_STEP)
    delta = -ADAM_LR * (m_hat / (jnp.sqrt(v_hat) + ADAM_EPS) + ADAM_WD * w)
    return delta, m, v


def _adamw(w, m, v, g_parts, name):
    r, c = w.shape
    tr = _row_tile(r)
    n = len(g_parts)

    def body(*refs):
        w_ref, m_ref, v_ref = refs[:3]
        g_refs = refs[3:3 + n]
        g_out, d_out, m_out, v_out = refs[3 + n:]
        g = g_refs[0][...]
        for ref in g_refs[1:]:
            g = g + ref[...]
        g_out[...] = g
        d_out[...], m_out[...], v_out[...] = _adamw_math(w_ref[...], g, m_ref[...], v_ref[...])

    blk = pl.BlockSpec((tr, c), lambda i: (i, 0))
    return pl.pallas_call(
        body, name=name, grid=(r // tr,), in_specs=[blk] * (3 + n), out_specs=[blk] * 4,
        out_shape=[jax.ShapeDtypeStruct((r, c), F32)] * 4, compiler_params=_params("parallel"),
    )(w, m, v, *g_parts)


def _adamw_small(total, w, m, v):
    sizes = [w[n].size for n in SMALL]
    flat = lambda d: [d[n].reshape(1, -1) for n in SMALL]
    k = len(SMALL)

    def body(*refs):
        total_ref, w_refs, m_refs, v_refs = refs[0], refs[1:1 + k], refs[1 + k:1 + 2 * k], refs[1 + 2 * k:1 + 3 * k]
        outs = refs[1 + 3 * k:]
        for i, size in enumerate(sizes):
            g = total_ref[i:i + 1, 0:size]
            outs[i][...] = g
            outs[k + i][...], outs[2 * k + i][...], outs[3 * k + i][...] = _adamw_math(
                w_refs[i][...], g, m_refs[i][...], v_refs[i][...])

    res = pl.pallas_call(
        body, name="adamw_small", out_shape=[jax.ShapeDtypeStruct((1, size), F32) for size in sizes] * 4,
        compiler_params=_params(),
    )(total, *flat(w), *flat(m), *flat(v))
    return [{n: res[j * k + i].reshape(w[n].shape) for i, n in enumerate(SMALL)} for j in range(4)]


SHARDED = ("w_in", "w_lora_up", "a_lora_up", "w_out_a", "w_out_b", "w_out")
ROW_SHARDED = ("w_out",)
SMALL = ("norm_g", "shift_mu", "w0", "a0", "k_k", "k_a", "r_k", "lnx_w", "lnx_b", "f_bias", "q_norm_g", "k_norm_g",
         "final_norm_g")
WEIGHTS = ("norm_g", "w_in", "shift_mu", "w_lora_up", "w0", "a_lora_up", "a0", "k_k", "k_a", "r_k", "lnx_w", "lnx_b",
           "f_bias", "q_norm_g", "k_norm_g", "w_out_a", "w_out_b", "w_out", "final_norm_g")
SLAB_ROWS = 16
SLAB_COLS = SEC


def _to_slab(named, extra=None):
    rows = [jnp.pad(named[n].reshape(1, -1), ((0, 0), (0, SLAB_COLS - named[n].size))) for n in SMALL]
    if extra is not None:
        rows.append(jnp.pad(extra.reshape(1, -1), ((0, 0), (0, SLAB_COLS - extra.size))))
    rows.append(jnp.zeros((SLAB_ROWS - len(rows), SLAB_COLS), F32))
    return jnp.concatenate(rows, axis=0)


def _by_chip(g, name):
    if name in ROW_SHARDED:
        return g.reshape(N_CHIPS, g.shape[0] // N_CHIPS, g.shape[1])
    r, c = g.shape
    return g.reshape(r, N_CHIPS, c // N_CHIPS).transpose(1, 0, 2)


def _from_chips(stack, name):
    if name in ROW_SHARDED:
        return stack.reshape(-1, stack.shape[2])
    _, r, c = stack.shape
    return stack.transpose(1, 0, 2).reshape(r, N_CHIPS * c)


def kernel(x, norm_g, w_in, shift_mu, w_lora_up, w0, a_lora_up, a0, k_k, k_a, r_k, lnx_w, lnx_b, f_bias, q_norm_g, k_norm_g, w_out_a, w_out_b, w_out, final_norm_g, loss_target, m_norm_g, m_w_in, m_shift_mu, m_w_lora_up, m_w0, m_a_lora_up, m_a0, m_k_k, m_k_a, m_r_k, m_lnx_w, m_lnx_b, m_f_bias, m_q_norm_g, m_k_norm_g, m_w_out_a, m_w_out_b, m_w_out, m_final_norm_g, v_norm_g, v_w_in, v_shift_mu, v_w_lora_up, v_w0, v_a_lora_up, v_a0, v_k_k, v_k_a, v_r_k, v_lnx_w, v_lnx_b, v_f_bias, v_q_norm_g, v_k_norm_g, v_w_out_a, v_w_out_b, v_w_out, v_final_norm_g):
    w = dict(norm_g=norm_g, w_in=w_in, shift_mu=shift_mu, w_lora_up=w_lora_up, w0=w0, a_lora_up=a_lora_up, a0=a0,
             k_k=k_k, k_a=k_a, r_k=r_k, lnx_w=lnx_w, lnx_b=lnx_b, f_bias=f_bias, q_norm_g=q_norm_g,
             k_norm_g=k_norm_g, w_out_a=w_out_a, w_out_b=w_out_b, w_out=w_out, final_norm_g=final_norm_g)
    m = dict(norm_g=m_norm_g, w_in=m_w_in, shift_mu=m_shift_mu, w_lora_up=m_w_lora_up, w0=m_w0,
             a_lora_up=m_a_lora_up, a0=m_a0, k_k=m_k_k, k_a=m_k_a, r_k=m_r_k, lnx_w=m_lnx_w, lnx_b=m_lnx_b,
             f_bias=m_f_bias, q_norm_g=m_q_norm_g, k_norm_g=m_k_norm_g, w_out_a=m_w_out_a, w_out_b=m_w_out_b,
             w_out=m_w_out, final_norm_g=m_final_norm_g)
    v = dict(norm_g=v_norm_g, w_in=v_w_in, shift_mu=v_shift_mu, w_lora_up=v_w_lora_up, w0=v_w0,
             a_lora_up=v_a_lora_up, a0=v_a0, k_k=v_k_k, k_a=v_k_a, r_k=v_r_k, lnx_w=v_lnx_w, lnx_b=v_lnx_b,
             f_bias=v_f_bias, q_norm_g=v_q_norm_g, k_norm_g=v_k_norm_g, w_out_a=v_w_out_a, w_out_b=v_w_out_b,
             w_out=v_w_out, final_norm_g=v_final_norm_g)
    shapes = {n: w[n].shape for n in WEIGHTS}

    shard = {n: w[n][0].astype(BF16) for n in SHARDED}
    late = ("w_out_a", "w_out_b", "w_out")
    loras = ("w_lora_up", "a_lora_up")
    w_in_head, w_in_tail = shard["w_in"][:, :A_TAIL], shard["w_in"][:, A_TAIL:]
    shard0, shard1_head, up_stack, aup_stack = _run_on_sequencer(_gather_exchange(
        [(0, shard["w_in"]), (1, w_in_head)], [shard[n] for n in loras], split=(0, 1)), "gather_early", 2)
    moments = (_row_major_copy(m["w_in"][0], "m_w_in_rows"), _row_major_copy(v["w_in"][0], "v_w_in_rows"))
    shard0, moments = lax.optimization_barrier((shard0, moments))
    w_a = jnp.concatenate([shard0, shard1_head], axis=1)

    def late_weights(arrived):
        shard1_tail, shard2, shard3 = arrived[:3]
        w_b = jnp.concatenate([shard1_tail, shard2[:, :B_TAIL], jnp.zeros((D_MODEL, SEC - FOX_REAL), BF16)], axis=1)
        w_g = jnp.concatenate([shard2[:, B_TAIL:], shard3], axis=1)
        return (w_b, w_g, *[_from_chips(s, n) for n, s in zip(late, arrived[3:])])

    own = {}
    cut = {"block0": lambda: own["dw_a"][:, :SHARD_COLS], "head1": lambda: own["dw_a"][:, SHARD_COLS:],
           "tail1": lambda: own["dw_b"][:, :B_HEAD],
           "block2": lambda: jnp.concatenate([own["dw_b"][:, B_HEAD:FOX_REAL], own["dw_g"][:, :G_HEAD]], axis=1),
           "block3": lambda: own["dw_g"][:, G_HEAD:]}

    def bwd_exchange(dw_b, dw_g, dwa, dwb, dwo):
        own.update(dw_b=dw_b, dw_g=dw_g)
        own.update({n: _by_chip(g, n) for n, g in zip(late, (dwa, dwb, dwo))})
        return _scatter_exchange([(1, cut["tail1"]().astype(BF16)), (2, cut["block2"]().astype(BF16)),
                                  (3, cut["block3"]().astype(BF16))], [own[n].astype(BF16) for n in late])

    def tail_exchange(dw_a, dw_up, da_up):
        own.update(dw_a=dw_a)
        own.update({n: _by_chip(g, n) for n, g in zip(loras, (dw_up, da_up))})
        return _scatter_exchange([(0, cut["block0"]().astype(BF16)), (1, cut["head1"]().astype(BF16))],
                                 [own[n].astype(BF16) for n in loras], via_neighbours=True)

    small = {n: w[n] for n in SMALL}
    loss_vec, grad_x, grads, sent, sent_last = _device_grads(
        x[0], loss_target[0], small, w_a, _from_chips(up_stack, "w_lora_up"), _from_chips(aup_stack, "a_lora_up"),
        late_weights, _gather_exchange([(1, w_in_tail), (2, shard["w_in"]), (3, shard["w_in"])], [shard[n] for n in late]),
        bwd_exchange, tail_exchange)

    total = _allreduce_small(_to_slab(grads, extra=loss_vec))
    loss = (0.5 / D_MODEL) * jnp.sum(total[len(SMALL)])
    out_g, out_d, out_m, out_v = _adamw_small(total, w, m, v)

    xpos, ypos, _ = _position()
    me = (2 * xpos + ypos).astype(jnp.int32).reshape(1)
    core_sum = {n: _sum4(own[n], r, me) for n, r in zip(late, sent[3:])}
    theirs = dict(zip(late, _swap_sibling([core_sum[n] for n in late], "swap_sibling_early")))

    def update(n):
        m_n, v_n = moments if n == "w_in" else (m[n][0], v[n][0])
        g, d, m2, v2 = _adamw(w[n][0], m_n, v_n, [core_sum[n], theirs[n]], "adamw_" + n)
        out_g[n], out_d[n], out_m[n], out_v[n] = (a.reshape(shapes[n]) for a in (g, d, m2, v2))

    for n in late:
        update(n)
    done = (out_d["norm_g"], [out_d[n] for n in late])
    sent_last, (out_d["norm_g"], new_d) = lax.optimization_barrier((sent_last, done))
    out_d.update(zip(late, new_d))
    core_sum["w_in"] = lax.switch(me[0], [
        lambda: _sum_block(cut["block0"](), sent_last[0]),
        lambda: jnp.concatenate([_sum_block(cut["head1"](), sent_last[1]), _sum_block(cut["tail1"](), sent[0])], axis=1),
        lambda: _sum_block(cut["block2"](), sent[1]),
        lambda: _sum_block(cut["block3"](), sent[2])])
    core_sum.update({n: _sum4(own[n], r, me) for n, r in zip(loras, sent_last[2:])})
    rest = ("w_in",) + loras
    theirs.update(zip(rest, _swap_sibling([core_sum[n] for n in rest], "swap_sibling")))
    for n in rest:
        update(n)

    return (loss, grad_x.reshape(x.shape), *[out_g[n] for n in WEIGHTS], *[out_d[n] for n in WEIGHTS],
            *[out_m[n] for n in WEIGHTS], *[out_v[n] for n in WEIGHTS])
```

```python
import functools
import math

import jax
import jax.numpy as jnp
from jax import lax
from jax.experimental import pallas as pl
from jax.experimental.pallas import tpu as pltpu
from jax.experimental.pallas import tpu_sc as plsc

F32 = jnp.float32
BF16 = jnp.bfloat16

D_MODEL = 1024
D_HALF = 512
HEAD = 64
N_HEADS = 8
LORA = 64
RWKV_COLS = 2176
FOX_REAL = 2056
SEC = 2176
GATE_COLS = 2048
IN_COLS = 6280
N_CHIPS = 4
SHARD_COLS = IN_COLS // N_CHIPS
A_TAIL = RWKV_COLS - SHARD_COLS
B_HEAD = SHARD_COLS - A_TAIL
B_TAIL = FOX_REAL - B_HEAD
G_HEAD = SHARD_COLS - B_TAIL
RMS_EPS = 1e-6
LNX_EPS = 64e-5
ATT_SCALE = HEAD ** -0.5
NEG = -1e30

ADAM_LR = 0.001
ADAM_B1 = 0.9
ADAM_B2 = 0.999
ADAM_EPS = 1e-08
ADAM_WD = 0.01
ADAM_STEP = 10

LANES = 128
SUBLANES = 8
VMEM_LIMIT = 56 * 1024 * 1024
MESH = pl.DeviceIdType.MESH


def _params(*sem):
    return pltpu.CompilerParams(dimension_semantics=sem if sem else None, vmem_limit_bytes=VMEM_LIMIT)


def _sigmoid(x):
    return 1.0 / (1.0 + jnp.exp(-x))


def _log_sigmoid(x):
    return jnp.minimum(x, 0.0) - jnp.log(1.0 + jnp.exp(-jnp.abs(x)))


def _head_ones():
    r = lax.broadcasted_iota(jnp.int32, (LANES, LANES), 0) >> 6
    c = lax.broadcasted_iota(jnp.int32, (LANES, LANES), 1) >> 6
    return (r == c).astype(BF16)


def _split3(x):
    hi = x.astype(BF16)
    r1 = x - hi.astype(F32)
    mid = r1.astype(BF16)
    lo = (r1 - mid.astype(F32)).astype(BF16)
    return hi, mid, lo


def _exact_dot(x, ones_bf16, ones_first=False):
    out = None
    for piece in _split3(x):
        if ones_first:
            t = jnp.dot(ones_bf16, piece, preferred_element_type=F32)
        else:
            t = jnp.dot(piece, ones_bf16, preferred_element_type=F32)
        out = t if out is None else out + t
    return out


def _head_sum(x, bd):
    n = x.shape[1] // LANES
    parts = [_exact_dot(x[:, i * LANES:(i + 1) * LANES], bd) for i in range(n)]
    return parts[0] if n == 1 else jnp.concatenate(parts, axis=1)


def _dot_nt(a, b):
    return lax.dot_general(a, b, (((1,), (1,)), ((), ())), preferred_element_type=F32)


def _dot_tn(a, b):
    return lax.dot_general(a, b, (((0,), (0,)), ((), ())), preferred_element_type=F32)


def _colsum(x):
    return jnp.sum(x, axis=0, keepdims=True)


def _matmul_tn_acc(at, b, name, tk=512):
    m, k = at.shape
    n = b.shape[1]

    def body(a_ref, b_ref, o_ref):
        j = pl.program_id(0)

        @pl.when(j == 0)
        def _():
            o_ref[...] = jnp.zeros_like(o_ref)

        o_ref[...] += jnp.dot(a_ref[...], b_ref[...].astype(BF16), preferred_element_type=F32)

    return pl.pallas_call(
        body, name=name, grid=(k // tk,),
        in_specs=[pl.BlockSpec((m, tk), lambda j: (0, j)), pl.BlockSpec((tk, n), lambda j: (j, 0))],
        out_specs=pl.BlockSpec((m, n), lambda j: (0, 0)),
        out_shape=jax.ShapeDtypeStruct((m, n), F32), compiler_params=_params("arbitrary"),
    )(at, b)


def _inproj_bwd(du_a, du_b, du_g, w_a, w_b, w_g, x, dx2, g, exchange=None, tm=256):
    s, d = x.shape
    nb = s // tm

    def body(*refs):
        ((da_ref, db_ref, dg_ref, wa_ref, wb_ref, wg_ref, x_ref, dx2_ref, g_ref), (gx_ref, gg_ref), _,
         moves) = _split_refs(refs, 9, 2, exchange)
        i = pl.program_id(0)
        if moves:
            moves.start(also=(i == 0))

        @pl.when(i == 0)
        def _():
            gg_ref[...] = jnp.zeros_like(gg_ref)

        dh = _dot_nt(da_ref[...].astype(BF16), wa_ref[...])
        dh += _dot_nt(db_ref[...].astype(BF16), wb_ref[...])
        dh += _dot_nt(dg_ref[...].astype(BF16), wg_ref[...])
        xv = x_ref[...]
        r = lax.rsqrt(jnp.mean(xv * xv, axis=-1, keepdims=True) + RMS_EPS)
        xh = xv * r
        gg_ref[...] += _colsum(dh * xh)
        dxh = dh * g_ref[...]
        gx_ref[...] = dx2_ref[...] + r * (dxh - xh * jnp.mean(dxh * xh, axis=-1, keepdims=True))
        if moves:
            moves.wait(also=(i == nb - 1))

    row = lambda w: pl.BlockSpec((tm, w), lambda i: (i, 0))
    full = lambda a: pl.BlockSpec(a.shape, lambda i: (0, 0))
    ex_in = exchange.operands if exchange else []
    ex_out = exchange.out_shapes if exchange else []
    res = pl.pallas_call(
        body, name="inproj_bwd", grid=(nb,),
        in_specs=[row(SEC), row(SEC), row(GATE_COLS), full(w_a), full(w_b), full(w_g), row(d), row(d), full(g)]
                 + [ANY] * len(ex_in),
        out_specs=[row(d), pl.BlockSpec((1, d), lambda i: (0, 0))] + [ANY] * len(ex_out),
        out_shape=[jax.ShapeDtypeStruct((s, d), F32), jax.ShapeDtypeStruct((1, d), F32)] + ex_out,
        scratch_shapes=exchange.scratch() if exchange else [],
        compiler_params=_params("arbitrary"),
    )(du_a, du_b, du_g, w_a, w_b, w_g, x, dx2, g, *ex_in)
    return res[0], res[1], list(res[2:])


def _rwkv_elementwise(ua, prev_row, first, mu, wl, w0, a0, kkw, kaw, bd):
    tm = ua.shape[0]
    rows = lax.broadcasted_iota(jnp.int32, (tm, 1), 0)
    prev = jnp.where(first, jnp.zeros_like(prev_row), prev_row)
    shifted = jnp.where(rows == 0, prev, pltpu.roll(ua, 1, 0))
    delta = shifted - ua
    us = ua + delta * mu
    r = us[:, 0:512]
    k0 = us[:, 512:1024]
    v = us[:, 1024:1536]
    lo = us[:, 1536:1664]
    gate = us[:, 1664:2176]
    lane = lax.broadcasted_iota(jnp.int32, (1, LANES), 1)
    th = jnp.tanh(lo)
    lin = jnp.where(lane < LORA, th, lo)
    ll = jnp.dot(lin.astype(BF16), wl, preferred_element_type=F32)
    sz = _sigmoid(w0 + ll[:, :512])
    e = sz * math.exp(-0.5)
    dec = jnp.exp(-e)
    a = _sigmoid(a0 + ll[:, 512:])
    kk0 = k0 * kkw
    ss = _head_sum(kk0 * kk0, bd)
    nrm = jnp.maximum(jnp.sqrt(ss), 1e-12)
    kk = kk0 / nrm
    k = k0 * (1.0 + (a - 1.0) * kaw)
    return dict(delta=delta, us=us, r=r, k0=k0, v=v, lo=lo, gate=gate, th=th, lin=lin, sz=sz, e=e, dec=dec,
                a=a, kk0=kk0, ss=ss, nrm=nrm, kk=kk, k=k)


def _rwkv_front(x, g, w_a, mu, wl, w0, a0, kkw, kaw, tm=256):
    s, d = x.shape

    def body(x_ref, g_ref, wa_ref, mu_ref, wl_ref, w0_ref, a0_ref, kkw_ref, kaw_ref,
             h_ref, ua_ref, r_ref, w_ref, k_ref, v_ref, a_ref, b_ref, gate_ref, last_row):
        i = pl.program_id(0)
        xv = x_ref[...]
        h = (xv * lax.rsqrt(jnp.mean(xv * xv, axis=-1, keepdims=True) + RMS_EPS) * g_ref[...]).astype(BF16)
        h_ref[...] = h
        ua = jnp.dot(h, wa_ref[...], preferred_element_type=F32)
        ua_ref[...] = ua

        @pl.when(i == 0)
        def _():
            last_row[...] = jnp.zeros_like(last_row)

        f = _rwkv_elementwise(ua, last_row[...], i == 0, mu_ref[...], wl_ref[...], w0_ref[...],
                              a0_ref[...], kkw_ref[...], kaw_ref[...], _head_ones())
        last_row[...] = ua[tm - 1:tm, :]
        r_ref[...] = f["r"]
        w_ref[...] = f["dec"]
        k_ref[...] = f["k"]
        v_ref[...] = f["v"]
        a_ref[...] = -f["kk"]
        b_ref[...] = f["kk"] * f["a"]
        gate_ref[...] = f["gate"]

    vec = lambda w: pl.BlockSpec((1, w), lambda i: (0, 0))
    row = lambda w: pl.BlockSpec((tm, w), lambda i: (i, 0))
    return pl.pallas_call(
        body, name="rwkv_front", grid=(s // tm,),
        in_specs=[row(d), vec(d), pl.BlockSpec(w_a.shape, lambda i: (0, 0), pipeline_mode=pl.Buffered(1)),
                  vec(SEC), pl.BlockSpec((LANES, 2 * D_HALF), lambda i: (0, 0)),
                  vec(D_HALF), vec(D_HALF), vec(D_HALF), vec(D_HALF)],
        out_specs=[row(d), row(SEC)] + [row(D_HALF)] * 7,
        out_shape=[jax.ShapeDtypeStruct((s, d), BF16), jax.ShapeDtypeStruct((s, SEC), F32)]
                  + [jax.ShapeDtypeStruct((s, D_HALF), F32)] * 7,
        scratch_shapes=[pltpu.VMEM((1, SEC), F32)],
        compiler_params=_params("arbitrary"),
    )(x, g, w_a, mu, wl, w0, a0, kkw, kaw)


SCAN_TB = 128
N_PAIRS = 4


def _pair_sum(x, left):
    s_l = jnp.sum(jnp.where(left, x, 0.0), axis=1, keepdims=True)
    s_r = jnp.sum(jnp.where(left, 0.0, x), axis=1, keepdims=True)
    return jnp.where(left, s_l, s_r)


def _pair_dot(x, row_l, row_r, left):
    s_l = jnp.sum(x * row_l, axis=1, keepdims=True)
    s_r = jnp.sum(x * row_r, axis=1, keepdims=True)
    return jnp.where(left, s_l, s_r)


def _halves(rows8):
    lane = lax.broadcasted_iota(jnp.int32, rows8.shape, 1)
    keep_left = (lane & (LANES - 1)) < HEAD
    return jnp.where(keep_left, rows8, 0.0), jnp.where(keep_left, 0.0, rows8)


def _quad_consts():
    lane = lax.broadcasted_iota(jnp.int32, (HEAD, 2 * LANES), 1)
    rowi = lax.broadcasted_iota(jnp.int32, (HEAD, 2 * LANES), 0)
    diag2 = rowi == (lane & (HEAD - 1))
    r = lax.broadcasted_iota(jnp.int32, (2 * LANES, 2 * LANES), 0) >> 6
    c = lax.broadcasted_iota(jnp.int32, (2 * LANES, 2 * LANES), 1) >> 6
    return diag2, (r == c).astype(BF16)


def _rows_to_columns(x8, diag2, bd2):
    lhs = jnp.concatenate([jnp.where(diag2, x8[i:i + 1], 0.0).astype(BF16) for i in range(SUBLANES)], axis=0)
    return jnp.dot(lhs, bd2, preferred_element_type=F32)


def _diag_rows(qtile, diag2, bd2, sub_row2):
    res = jnp.dot(qtile, bd2, preferred_element_type=F32)
    out = jnp.zeros((SUBLANES, 2 * LANES), F32)
    for i in range(SUBLANES):
        out = jnp.where(sub_row2 == i, _colsum(jnp.where(diag2, res[i * HEAD:(i + 1) * HEAD], 0.0)), out)
    return out


def _store_tile(qbuf, slot, p, i, x):
    qbuf[slot, p // 2, i * HEAD:(i + 1) * HEAD, (p % 2) * LANES:(p % 2 + 1) * LANES] = x.astype(BF16)


def _left_half():
    return lax.broadcasted_iota(jnp.int32, (HEAD, LANES), 1) < HEAD


def _split_refs(refs, n_rows, n_out, exchange):
    n_in = len(exchange.operands) if exchange else 0
    n_ex_out = len(exchange.out_shapes) if exchange else 0
    refs = list(refs)
    rows, refs = refs[:n_rows], refs[n_rows:]
    ex_in, refs = refs[:n_in], refs[n_in:]
    outs, refs = refs[:n_out], refs[n_out:]
    ex_out, refs = refs[:n_ex_out], refs[n_ex_out:]
    scratch, sems = (refs[:-3], refs[-3:]) if exchange else (refs, None)
    moves = exchange.moves(ex_in, ex_out, sems) if exchange else None
    return rows, outs, scratch, moves


def _wkv_fwd(r, w, k, a, b, v, exchange=None):
    s = r.shape[0]
    tb = SCAN_TB
    nb = s // tb

    def body(*refs):
        (r_ref, w_ref, k_ref, a_ref, b_ref, v_ref), (y_ref, st_ref), (state, vbuf, qbuf), moves = _split_refs(
            refs, 6, 2, exchange)
        g = pl.program_id(0)
        if moves:
            moves.start(also=(g == 0))

        @pl.when(g == 0)
        def _():
            state[...] = jnp.zeros_like(state)
            qbuf[...] = jnp.zeros_like(qbuf)

        left = _left_half()
        diag2, bd2 = _quad_consts()
        sub_row2 = lax.broadcasted_iota(jnp.int32, (SUBLANES, 2 * LANES), 0)
        groups = tb // SUBLANES
        quads = [slice(g2 * 2 * LANES, (g2 + 1) * 2 * LANES) for g2 in range(2)]

        def rows_of(q):
            return pl.ds(pl.multiple_of(q * SUBLANES, SUBLANES), SUBLANES)

        def v_tiles(q, slot):
            v8 = v_ref[rows_of(q), :]
            for g2 in range(2):
                vbuf[slot, g2] = _rows_to_columns(v8[:, quads[g2]], diag2, bd2)

        def chain(q, slot):
            rows8 = rows_of(q)
            a8, w8, b8, k8, r8 = (x[rows8, :] for x in (a_ref, w_ref, b_ref, k_ref, r_ref))
            pairs = [slice(p * LANES, (p + 1) * LANES) for p in range(N_PAIRS)]
            a_next = pltpu.roll(a8, SUBLANES - 1, 0)
            (a8_l, a8_r), (wa8_l, wa8_r) = _halves(a8), _halves(w8 * a_next)
            ba8 =jnp.concatenate([_pair_sum(b8[:, pr] * a_next[:, pr], left[0:SUBLANES]) for pr in pairs], axis=1)
            ka8 = jnp.concatenate([_pair_sum(k8[:, pr] * a_next[:, pr], left[0:SUBLANES]) for pr in pairs], axis=1)
            sp = [state[p] for p in range(N_PAIRS)]
            for i in range(0, SUBLANES, 2):
                r0, r1 = slice(i, i + 1), slice(i + 1, i + 2)
                sums = [(_pair_dot(sp[p], a8_l[r0, pairs[p]], a8_r[r0, pairs[p]], left),
                         _pair_dot(sp[p], wa8_l[r0, pairs[p]], wa8_r[r0, pairs[p]], left)) for p in range(N_PAIRS)]
                sa0, sa1 = [s[0] for s in sums], [s[1] for s in sums]
                for p in range(N_PAIRS):
                    pr = pairs[p]
                    inner = slice((p % 2) * LANES, (p % 2 + 1) * LANES)
                    vt0 = vbuf[slot, p // 2, i * HEAD:(i + 1) * HEAD, inner]
                    vt1 = vbuf[slot, p // 2, (i + 1) * HEAD:(i + 2) * HEAD, inner]
                    sa_next = sa1[p] + sa0[p] * ba8[r0, pr] + vt0 * ka8[r0, pr]
                    s1 = sp[p] * w8[r0, pr] + sa0[p] * b8[r0, pr] + vt0 * k8[r0, pr]
                    st_ref[q * SUBLANES + i, p] = s1
                    _store_tile(qbuf, slot, p, i, s1 * r8[r0, pr])
                    s2 = s1 * w8[r1, pr] + sa_next * b8[r1, pr] + vt1 * k8[r1, pr]
                    st_ref[q * SUBLANES + i + 1, p] = s2
                    _store_tile(qbuf, slot, p, i + 1, s2 * r8[r1, pr])
                    sp[p] = s2
            for p in range(N_PAIRS):
                state[p] = sp[p]

        def y_rows(q, slot):
            for g2 in range(2):
                y_ref[rows_of(q), quads[g2]] = _diag_rows(qbuf[slot, g2], diag2, bd2, sub_row2)

        v_tiles(0, 0)

        def two_groups(j, carry):
            q0 = 2 * j
            v_tiles(q0 + 1, 1)
            chain(q0, 0)
            y_rows(jnp.maximum(q0 - 1, 0), 1)
            v_tiles(jnp.minimum(q0 + 2, groups - 1), 0)
            chain(q0 + 1, 1)
            y_rows(q0, 0)
            return carry

        lax.fori_loop(0, groups // 2, two_groups, 0)
        y_rows(groups - 1, 1)
        if moves:
            moves.wait(also=(g == nb - 1))

    rows = pl.BlockSpec((tb, D_HALF), lambda g: (g, 0))
    ex_in = exchange.operands if exchange else []
    ex_out = exchange.out_shapes if exchange else []
    res = pl.pallas_call(
        body, name="wkv_fwd", grid=(nb,),
        in_specs=[rows] * 6 + [ANY] * len(ex_in),
        out_specs=[rows, pl.BlockSpec((tb, N_PAIRS, HEAD, LANES), lambda g: (g, 0, 0, 0))] + [ANY] * len(ex_out),
        out_shape=[jax.ShapeDtypeStruct((s, D_HALF), F32),
                   jax.ShapeDtypeStruct((s, N_PAIRS, HEAD, LANES), F32)] + ex_out,
        scratch_shapes=[pltpu.VMEM((N_PAIRS, HEAD, LANES), F32),
                        pltpu.VMEM((2, 2, SUBLANES * HEAD, 2 * LANES), F32),
                        pltpu.VMEM((2, 2, SUBLANES * HEAD, 2 * LANES), BF16)]
                       + (exchange.scratch() if exchange else []),
        compiler_params=_params("arbitrary"),
    )(r, w, k, a, b, v, *ex_in)
    return res[0], res[1], list(res[2:])


def _wkv_bwd(r, w, k, a, b, v, dy, st, exchange=None):
    s = r.shape[0]
    tb = SCAN_TB
    nb = s // tb

    def body(*refs):
        ((r_ref, w_ref, k_ref, a_ref, b_ref, v_ref, dy_ref, st_ref, before_ref),
         (dr_ref, dw_ref, dk_ref, dv_ref, da_ref, db_ref), (dstate, vbuf, qbuf, sbuf),
         moves) = _split_refs(refs, 9, 6, exchange)
        g = pl.program_id(0)
        first_block = g == nb - 1
        if moves:
            moves.start(also=(g == 0))

        @pl.when(g == 0)
        def _():
            dstate[...] = jnp.zeros_like(dstate)
            qbuf[...] = jnp.zeros_like(qbuf)

        left = _left_half()
        diag2, bd2 = _quad_consts()
        sub_row = lax.broadcasted_iota(jnp.int32, (SUBLANES, LANES), 0)
        sub_row2 = lax.broadcasted_iota(jnp.int32, (SUBLANES, 2 * LANES), 0)
        groups = tb // SUBLANES
        quads = [slice(g2 * 2 * LANES, (g2 + 1) * 2 * LANES) for g2 in range(2)]
        row_refs = (dr_ref, dw_ref, dk_ref, da_ref, db_ref)

        def rows_of(q):
            return pl.ds(pl.multiple_of(q * SUBLANES, SUBLANES), SUBLANES)

        def state_before(q, i, p):
            if i > 0:
                return st_ref[q * SUBLANES + i - 1, p]
            return jnp.where(q == 0, jnp.where(first_block, 0.0, before_ref[0, p]),
                             st_ref[jnp.maximum(q * SUBLANES - 1, 0), p])

        def column_tiles(q, slot):
            rows8 = rows_of(q)
            for kind, ref in enumerate((v_ref, dy_ref)):
                x8 = ref[rows8, :]
                for g2 in range(2):
                    vbuf[slot, kind, g2] = _rows_to_columns(x8[:, quads[g2]], diag2, bd2)
            a8 = a_ref[rows8, :]
            for i in range(SUBLANES):
                for p in range(N_PAIRS):
                    _store_tile(sbuf, 0, p, i, state_before(q, i, p) * a8[i:i + 1, p * LANES:(p + 1) * LANES])
            for g2 in range(2):
                vbuf[slot, 2, g2] = jnp.dot(sbuf[0, g2], bd2, preferred_element_type=F32)

        def chain(q, slot):
            rows8 = rows_of(q)
            a8, w8, b8, k8, r8 = (x[rows8, :] for x in (a_ref, w_ref, b_ref, k_ref, r_ref))
            b8_l, b8_r = _halves(b8)
            dsp = [dstate[p] for p in range(N_PAIRS)]
            outs = [[jnp.zeros((SUBLANES, LANES), F32) for _ in row_refs] for _ in range(N_PAIRS)]
            after = [st_ref[q * SUBLANES + SUBLANES - 1, p] for p in range(N_PAIRS)]
            for i in reversed(range(SUBLANES)):
                row = slice(i, i + 1)
                pl_ = [slice(p * LANES, (p + 1) * LANES) for p in range(N_PAIRS)]
                tile = [(p // 2, slice(i * HEAD, (i + 1) * HEAD), slice((p % 2) * LANES, (p % 2 + 1) * LANES))
                        for p in range(N_PAIRS)]
                sp = [state_before(q, i, p) for p in range(N_PAIRS)]
                dyt = [vbuf[(slot, 1) + tile[p]] for p in range(N_PAIRS)]
                ds = [dsp[p] + dyt[p] * r8[row, pl_[p]] for p in range(N_PAIRS)]
                dsa = [_pair_dot(ds[p], b8_l[row, pl_[p]], b8_r[row, pl_[p]], left) for p in range(N_PAIRS)]
                sa = [vbuf[(slot, 2) + tile[p]] for p in range(N_PAIRS)]
                for p in range(N_PAIRS):
                    ar, wr, br, kr = (x[row, pl_[p]] for x in (a8, w8, b8, k8))
                    vt = vbuf[(slot, 0) + tile[p]]
                    dsp[p] = ds[p] * wr + dsa[p] * ar
                    new = (_colsum(after[p] * dyt[p]), _colsum(ds[p] * sp[p]), _colsum(ds[p] * vt),
                           _colsum(sp[p] * dsa[p]), _colsum(ds[p] * sa[p]))
                    outs[p] = [jnp.where(sub_row == i, n, o) for n, o in zip(new, outs[p])]
                    _store_tile(qbuf, slot, p, i, ds[p] * kr)
                after = sp
            for p in range(N_PAIRS):
                dstate[p] = dsp[p]
                for ref, o in zip(row_refs, outs[p]):
                    ref[rows8, p * LANES:(p + 1) * LANES] = o

        def dv_rows(q, slot):
            for g2 in range(2):
                dv_ref[rows_of(q), quads[g2]] = _diag_rows(qbuf[slot, g2], diag2, bd2, sub_row2)

        column_tiles(groups - 1, 0)

        def two_groups(j, carry):
            q0 = groups - 1 - 2 * j
            column_tiles(q0 - 1, 1)
            chain(q0, 0)
            dv_rows(jnp.minimum(q0 + 1, groups - 1), 1)
            column_tiles(jnp.maximum(q0 - 2, 0), 0)
            chain(q0 - 1, 1)
            dv_rows(q0, 0)
            return carry

        lax.fori_loop(0, groups // 2, two_groups, 0)
        dv_rows(0, 1)
        if moves:
            moves.wait(also=(g == nb - 1))

    rows = pl.BlockSpec((tb, D_HALF), lambda g: (nb - 1 - g, 0))
    ex_in = exchange.operands if exchange else []
    ex_out = exchange.out_shapes if exchange else []
    res = pl.pallas_call(
        body, name="wkv_bwd", grid=(nb,),
        in_specs=[rows] * 7 + [pl.BlockSpec((tb, N_PAIRS, HEAD, LANES), lambda g: (nb - 1 - g, 0, 0, 0)),
                               pl.BlockSpec((1, N_PAIRS, HEAD, LANES),
                                            lambda g: (jnp.maximum((nb - 1 - g) * tb - 1, 0), 0, 0, 0))]
                 + [ANY] * len(ex_in),
        out_specs=[rows] * 6 + [ANY] * len(ex_out),
        out_shape=[jax.ShapeDtypeStruct((s, D_HALF), F32)] * 6 + ex_out,
        scratch_shapes=[pltpu.VMEM((N_PAIRS, HEAD, LANES), F32),
                        pltpu.VMEM((2, 3, 2, SUBLANES * HEAD, 2 * LANES), F32),
                        pltpu.VMEM((2, 2, SUBLANES * HEAD, 2 * LANES), BF16),
                        pltpu.VMEM((1, 2, SUBLANES * HEAD, 2 * LANES), BF16)]
                       + (exchange.scratch() if exchange else []),
        compiler_params=_params("arbitrary"),
    )(r, w, k, a, b, v, dy, st, st, *ex_in)
    return list(res[:6]), list(res[6:])


def _rwkv_post_math(y, r, k, v, gate, lw, lb, rk, bd):
    mean = _head_sum(y, bd) * (1.0 / HEAD)
    yc = y - mean
    var = _head_sum(yc * yc, bd) * (1.0 / HEAD)
    rstd = lax.rsqrt(var + LNX_EPS)
    yn = yc * rstd
    rkk = _head_sum(r * k * rk, bd)
    sg = _sigmoid(gate)
    pre = yn * lw + lb + rkk * v
    return yn, rstd, rkk, sg, pre


def _rwkv_prep_bwd(u_a, h_t, grads, mu, wl, w0, a0, kkw, kaw, tm=256):
    s = u_a.shape[0]
    nb = s // tm
    d = h_t.shape[0]

    def body(ua_ref, prev_ref, ht_ref, drs_ref, dws_ref, dks_ref, dvs_ref, das_ref, dbs_ref, drb_ref, dkb_ref, dvb_ref,
             dgt_ref, mu_ref, wl_ref, w0_ref, a0_ref, kkw_ref, kaw_ref,
             du_ref, dwa_ref, dmu_ref, dwl_ref, dw0_ref, da0_ref, dkkw_ref, dkaw_ref, carry):
        i = pl.program_id(0)

        @pl.when(i == 0)
        def _():
            carry[...] = jnp.zeros_like(carry)
            for ref in (dwa_ref, dmu_ref, dwl_ref, dw0_ref, da0_ref, dkkw_ref, dkaw_ref):
                ref[...] = jnp.zeros_like(ref)

        bd = _head_ones()
        mu_v, wl_v, kkw_v, kaw_v = mu_ref[...], wl_ref[...], kkw_ref[...], kaw_ref[...]
        f = _rwkv_elementwise(ua_ref[...], prev_ref[7:8, :], i == nb - 1, mu_v, wl_v, w0_ref[...],
                              a0_ref[...], kkw_v, kaw_v, bd)
        a, kk, k0 = f["a"], f["kk"], f["k0"]
        dk = dks_ref[...] + dkb_ref[...]
        dbs = dbs_ref[...]
        dkk = dbs * a - das_ref[...]
        da = dbs * kk + dk * k0 * kaw_v
        dk0 = dk * (1.0 + (a - 1.0) * kaw_v)
        dkaw_ref[...] += _colsum(dk * k0 * (a - 1.0))
        inv = 1.0 / f["nrm"]
        proj = _head_sum(dkk * kk, bd)
        dkk0 = jnp.where(f["ss"] > 1e-24, (dkk - kk * proj) * inv, dkk * inv)
        dk0 = dk0 + dkk0 * kkw_v
        dkkw_ref[...] += _colsum(dkk0 * k0)
        dza = da * a * (1.0 - a)
        da0_ref[...] += _colsum(dza)
        dz = -dws_ref[...] * f["dec"] * f["e"] * (1.0 - f["sz"])
        dw0_ref[...] += _colsum(dz)
        dll = jnp.concatenate([dz, dza], axis=1).astype(BF16)
        dwl_ref[...] += _dot_tn(f["lin"].astype(BF16), dll)
        dlin = _dot_nt(dll, wl_v)
        lane = lax.broadcasted_iota(jnp.int32, (1, LANES), 1)
        th = f["th"]
        dlo = jnp.where(lane < LORA, dlin * (1.0 - th * th), dlin)
        dus = jnp.concatenate([drs_ref[...] + drb_ref[...], dk0, dvs_ref[...] + dvb_ref[...], dlo, dgt_ref[...]],
                              axis=1)
        dmu_ref[...] += _colsum(dus * f["delta"])
        g1 = dus * mu_v
        rows = lax.broadcasted_iota(jnp.int32, (tm, 1), 0)
        up = jnp.where(rows == tm - 1, carry[...], pltpu.roll(g1, tm - 1, 0))
        dua = dus - g1 + up
        du_ref[...] = dua
        dwa_ref[...] += jnp.dot(ht_ref[...], dua.astype(BF16), preferred_element_type=F32)
        carry[...] = g1[0:1, :]

    rev = lambda w: pl.BlockSpec((tm, w), lambda i: (nb - 1 - i, 0))
    vec = lambda w: pl.BlockSpec((1, w), lambda i: (0, 0))
    wl_spec = pl.BlockSpec((LANES, 2 * D_HALF), lambda i: (0, 0))
    return pl.pallas_call(
        body, name="rwkv_prep_bwd", grid=(nb,),
        in_specs=[rev(SEC), pl.BlockSpec((8, SEC), lambda i: (jnp.maximum((nb - 1 - i) * (tm // 8) - 1, 0), 0)),
                  pl.BlockSpec((d, tm), lambda i: (0, nb - 1 - i))]
                 + [rev(D_HALF)] * 10 + [vec(SEC), wl_spec] + [vec(D_HALF)] * 4,
        out_specs=[rev(SEC), pl.BlockSpec((d, SEC), lambda i: (0, 0)), vec(SEC), wl_spec] + [vec(D_HALF)] * 4,
        out_shape=[jax.ShapeDtypeStruct((s, SEC), F32), jax.ShapeDtypeStruct((d, SEC), F32),
                   jax.ShapeDtypeStruct((1, SEC), F32),
                   jax.ShapeDtypeStruct((LANES, 2 * D_HALF), F32)] + [jax.ShapeDtypeStruct((1, D_HALF), F32)] * 4,
        scratch_shapes=[pltpu.VMEM((1, SEC), F32)],
        compiler_params=_params("arbitrary"),
    )(u_a, u_a, h_t, *grads, mu, wl, w0, a0, kkw, kaw)


def _tri(tm, lower):
    r = lax.broadcasted_iota(jnp.int32, (tm, tm), 0)
    c = lax.broadcasted_iota(jnp.int32, (tm, tm), 1)
    return ((r >= c) if lower else (r <= c)).astype(BF16)


def _head_rms(x, g, bd):
    rinv = lax.rsqrt(_head_sum(x * x, bd) * (1.0 / HEAD) + RMS_EPS)
    xh = x * rinv
    return xh, rinv, xh * g


def _fox_front(h, w_b, fb, qg, kg, tm=256):
    s, d = h.shape

    def body(h_ref, wb_ref, fb_ref, qg_ref, kg_ref, ub_ref, q_ref, k_ref, v_ref, cc_ref, cr_ref, carry):
        i = pl.program_id(0)

        @pl.when(i == 0)
        def _():
            carry[...] = jnp.zeros_like(carry)

        ub_ref[...] = jnp.dot(h_ref[...], wb_ref[...], preferred_element_type=F32)
        bd = _head_ones()
        _, _, qn = _head_rms(ub_ref[:, 0:512], qg_ref[...], bd)
        _, _, kn = _head_rms(ub_ref[:, 512:1024], kg_ref[...], bd)
        q_ref[...] = (qn * ATT_SCALE).astype(BF16)
        k_ref[...] = kn.astype(BF16)
        v_ref[...] = ub_ref[:, 1024:1536].astype(BF16)
        lane = lax.broadcasted_iota(jnp.int32, (1, LANES), 1)
        logf = jnp.where(lane < N_HEADS, _log_sigmoid(ub_ref[:, 2048:2176] + fb_ref[...]), 0.0)
        cum = _exact_dot(logf, _tri(tm, True), ones_first=True) + carry[...]
        for h in range(N_HEADS):
            cc_ref[h] = jnp.broadcast_to(cum[:, h:h + 1], (tm, LANES))
        cr_ref[...] = jnp.transpose(cum)[0:N_HEADS, :]
        carry[...] = cum[tm - 1:tm, :]

    blk = pl.BlockSpec((tm, D_HALF), lambda i: (i, 0))
    return pl.pallas_call(
        body, name="fox_front", grid=(s // tm,),
        in_specs=[pl.BlockSpec((tm, d), lambda i: (i, 0)),
                  pl.BlockSpec(w_b.shape, lambda i: (0, 0), pipeline_mode=pl.Buffered(1)),
                  pl.BlockSpec((1, LANES), lambda i: (0, 0)),
                  pl.BlockSpec((1, D_HALF), lambda i: (0, 0)), pl.BlockSpec((1, D_HALF), lambda i: (0, 0))],
        out_specs=[pl.BlockSpec((tm, SEC), lambda i: (i, 0)), blk, blk, blk,
                   pl.BlockSpec((N_HEADS, tm, LANES), lambda i: (0, i, 0)), pl.BlockSpec((N_HEADS, tm), lambda i: (0, i))],
        out_shape=[jax.ShapeDtypeStruct((s, SEC), F32)] + [jax.ShapeDtypeStruct((s, D_HALF), BF16)] * 3
                  + [jax.ShapeDtypeStruct((N_HEADS, s, LANES), F32), jax.ShapeDtypeStruct((N_HEADS, s), F32)],
        scratch_shapes=[pltpu.VMEM((1, LANES), F32)],
        compiler_params=_params("arbitrary"),
    )(h, w_b, fb, qg, kg)


ATT_T = 256


def _tiles(nblk, by_query):
    if by_query:
        pairs = [(i, j) for i in range(nblk) for j in range(i + 1)]
    else:
        pairs = [(i, j) for j in range(nblk) for i in range(j, nblk)]
    return (jnp.asarray([p[0] for p in pairs], jnp.int32), jnp.asarray([p[1] for p in pairs], jnp.int32))


def _attn_fwd(q, k, v, cc, cr):
    s = q.shape[0]
    t = ATT_T
    nblk = s // t

    def body(qi_ref, kj_ref, q_ref, k_ref, v_ref, cc_ref, cr_ref, o_ref, lse_ref, m_sc, l_sc, acc_sc):
        i = qi_ref[pl.program_id(0)]
        j = kj_ref[pl.program_id(0)]

        @pl.when(j == 0)
        def _():
            m_sc[...] = jnp.full_like(m_sc, NEG)
            l_sc[...] = jnp.zeros_like(l_sc)
            acc_sc[...] = jnp.zeros_like(acc_sc)

        def tile(on_diagonal):
            causal = _causal_tile(t) if on_diagonal else None
            left = lax.broadcasted_iota(jnp.int32, (1, LANES), 1) < HEAD
            for p in range(N_PAIRS):
                lanes = slice(p * LANES, (p + 1) * LANES)
                q2, k2, v2 = q_ref[:, lanes], k_ref[:, lanes], v_ref[:, lanes]
                acc2 = acc_sc[:, lanes]
                for e in range(2):
                    h = 2 * p + e
                    msk = left if e == 0 else jnp.logical_not(left)
                    sc = _dot_nt(jnp.where(msk, q2, jnp.zeros_like(q2)), k2)
                    sc = sc + (_wide(cc_ref[h]) - cr_ref[h:h + 1, :])
                    if on_diagonal:
                        sc = jnp.where(causal, sc, NEG)
                    m_prev = m_sc[h]
                    m_new = jnp.maximum(m_prev, jnp.max(sc, axis=1, keepdims=True))
                    alpha = jnp.exp(m_prev - m_new)
                    pm = jnp.exp(sc - _wide(m_new))
                    l_sc[h] = alpha * l_sc[h] + jnp.sum(pm, axis=1, keepdims=True)
                    m_sc[h] = m_new
                    pv = jnp.dot(pm.astype(BF16), v2, preferred_element_type=F32)
                    acc2 = jnp.where(msk, alpha * acc2 + pv, acc2)
                acc_sc[:, lanes] = acc2

        pl.when(j < i)(functools.partial(tile, False))
        pl.when(j == i)(functools.partial(tile, True))

        @pl.when(j == i)
        def _():
            left = lax.broadcasted_iota(jnp.int32, (1, LANES), 1) < HEAD
            for p in range(N_PAIRS):
                lanes = slice(p * LANES, (p + 1) * LANES)
                inv = jnp.where(left, 1.0 / l_sc[2 * p], 1.0 / l_sc[2 * p + 1])
                o_ref[:, lanes] = acc_sc[:, lanes] * inv
            for h in range(N_HEADS):
                lse_ref[h] = m_sc[h] + jnp.log(l_sc[h])

    qi, kj = _tiles(nblk, by_query=True)
    qblk = pl.BlockSpec((t, D_HALF), lambda n, qi, kj: (qi[n], 0))
    kblk = pl.BlockSpec((t, D_HALF), lambda n, qi, kj: (kj[n], 0))
    qrep = pl.BlockSpec((N_HEADS, t, LANES), lambda n, qi, kj: (0, qi[n], 0))
    return pl.pallas_call(
        body, name="fox_attn_fwd",
        grid_spec=pltpu.PrefetchScalarGridSpec(
            num_scalar_prefetch=2, grid=(qi.shape[0],),
            in_specs=[qblk, kblk, kblk, qrep, pl.BlockSpec((N_HEADS, t), lambda n, qi, kj: (0, kj[n]))],
            out_specs=[qblk, qrep],
            scratch_shapes=[pltpu.VMEM((N_HEADS, t, LANES), F32), pltpu.VMEM((N_HEADS, t, LANES), F32),
                            pltpu.VMEM((t, D_HALF), F32)]),
        out_shape=[jax.ShapeDtypeStruct((s, D_HALF), F32), jax.ShapeDtypeStruct((N_HEADS, s, LANES), F32)],
        compiler_params=_params("arbitrary"),
    )(qi, kj, q, k, v, cc, cr)


def _causal_tile(t):
    return lax.broadcasted_iota(jnp.int32, (t, t), 0) >= lax.broadcasted_iota(jnp.int32, (t, t), 1)


def _wide(x):
    return jnp.concatenate([x, x], axis=1)


def _attn_probs(q2, k2, v2, do2, msk, causal, bias, lse_rows):
    zero = jnp.zeros_like(q2)
    qh = jnp.where(msk, q2, zero)
    doh = jnp.where(msk, do2, zero)
    sc = _dot_nt(qh, k2) + bias
    if causal is not None:
        sc = jnp.where(causal, sc, NEG)
    pm = jnp.exp(sc - _wide(lse_rows))
    dp = _dot_nt(doh, v2)
    return qh, doh, pm, dp


def _attn_bwd_rowdot(q, k, v, do, lse, cc, cr):
    s = q.shape[0]
    t = ATT_T
    nblk = s // t

    def body(qi_ref, kj_ref, q_ref, k_ref, v_ref, do_ref, lse_ref, cc_ref, cr_ref, dd_ref, acc):
        i = qi_ref[pl.program_id(0)]
        j = kj_ref[pl.program_id(0)]

        @pl.when(j == 0)
        def _():
            acc[...] = jnp.zeros_like(acc)

        def tile(on_diagonal):
            causal = _causal_tile(t) if on_diagonal else None
            left = lax.broadcasted_iota(jnp.int32, (1, LANES), 1) < HEAD
            for p in range(N_PAIRS):
                lanes = slice(p * LANES, (p + 1) * LANES)
                q2, k2, v2, do2 = q_ref[:, lanes], k_ref[:, lanes], v_ref[:, lanes], do_ref[:, lanes]
                for e in range(2):
                    h = 2 * p + e
                    msk = left if e == 0 else jnp.logical_not(left)
                    bias = _wide(cc_ref[h]) - cr_ref[h:h + 1, :]
                    _, _, pm, dp = _attn_probs(q2, k2, v2, do2, msk, causal, bias, lse_ref[h])
                    acc[h] += jnp.sum(pm * dp, axis=1, keepdims=True)

        pl.when(j < i)(functools.partial(tile, False))
        pl.when(j == i)(functools.partial(tile, True))

        @pl.when(j == i)
        def _():
            dd_ref[...] = acc[...]

    qi, kj = _tiles(nblk, by_query=True)
    qblk = pl.BlockSpec((t, D_HALF), lambda n, qi, kj: (qi[n], 0))
    qcol = pl.BlockSpec((N_HEADS, t, LANES), lambda n, qi, kj: (0, qi[n], 0))
    kblk = pl.BlockSpec((t, D_HALF), lambda n, qi, kj: (kj[n], 0))
    return pl.pallas_call(
        body, name="fox_attn_rowdot",
        grid_spec=pltpu.PrefetchScalarGridSpec(
            num_scalar_prefetch=2, grid=(qi.shape[0],),
            in_specs=[qblk, kblk, kblk, qblk, qcol, qcol, pl.BlockSpec((N_HEADS, t), lambda n, qi, kj: (0, kj[n]))],
            out_specs=qcol, scratch_shapes=[pltpu.VMEM((N_HEADS, t, LANES), F32)]),
        out_shape=jax.ShapeDtypeStruct((N_HEADS, s, LANES), F32),
        compiler_params=_params("arbitrary"),
    )(qi, kj, q, k, v, do, lse, cc, cr)


def _attn_bwd(q, k, v, do, lse, dd, cc, cr):
    s = q.shape[0]
    t = ATT_T
    nblk = s // t

    def body(qi_ref, kj_ref, q_ref, k_ref, v_ref, do_ref, lse_ref, dd_ref, cc_ref, cr_ref,
             dq_ref, dk_ref, dv_ref, dcr_ref, dk_sc, dv_sc, dcr_sc):
        i = qi_ref[pl.program_id(0)]
        j = kj_ref[pl.program_id(0)]

        @pl.when(pl.program_id(0) == 0)
        def _():
            dq_ref[...] = jnp.zeros_like(dq_ref)

        @pl.when(i == j)
        def _():
            dk_sc[...] = jnp.zeros_like(dk_sc)
            dv_sc[...] = jnp.zeros_like(dv_sc)
            dcr_sc[...] = jnp.zeros_like(dcr_sc)

        def tile(on_diagonal):
            causal = _causal_tile(t) if on_diagonal else None
            left = lax.broadcasted_iota(jnp.int32, (1, LANES), 1) < HEAD
            qrows = pl.ds(pl.multiple_of(i * t, t), t)
            for p in range(N_PAIRS):
                lanes = slice(p * LANES, (p + 1) * LANES)
                q2, k2, v2, do2 = q_ref[:, lanes], k_ref[:, lanes], v_ref[:, lanes], do_ref[:, lanes]
                zero = jnp.zeros_like(q2)
                dq2 = jnp.zeros((t, LANES), F32)
                dk2 = jnp.zeros((t, LANES), F32)
                dv2 = jnp.zeros((t, LANES), F32)
                for e in range(2):
                    h = 2 * p + e
                    msk = left if e == 0 else jnp.logical_not(left)
                    bias = _wide(cc_ref[h]) - cr_ref[h:h + 1, :]
                    qh, doh, pm, dp = _attn_probs(q2, k2, v2, do2, msk, causal, bias, lse_ref[h])
                    dsc = pm * (dp - _wide(dd_ref[h]))
                    dsb = dsc.astype(BF16)
                    dv2 += _dot_tn(pm.astype(BF16), doh)
                    dk2 += _dot_tn(dsb, qh)
                    dq2 += jnp.dot(dsb, jnp.where(msk, k2, zero), preferred_element_type=F32)
                    dcr_sc[h:h + 1, :] += -_colsum(dsc)
                dq_ref[qrows, lanes] += dq2 * ATT_SCALE
                dk_sc[:, lanes] += dk2
                dv_sc[:, lanes] += dv2

        pl.when(i > j)(functools.partial(tile, False))
        pl.when(i == j)(functools.partial(tile, True))

        @pl.when(i == nblk - 1)
        def _():
            dk_ref[...] = dk_sc[...]
            dv_ref[...] = dv_sc[...]
            dcr_ref[...] = dcr_sc[...]

    qi, kj = _tiles(nblk, by_query=False)
    qblk = pl.BlockSpec((t, D_HALF), lambda n, qi, kj: (qi[n], 0))
    qcol = pl.BlockSpec((N_HEADS, t, LANES), lambda n, qi, kj: (0, qi[n], 0))
    kblk = pl.BlockSpec((t, D_HALF), lambda n, qi, kj: (kj[n], 0))
    krow = pl.BlockSpec((N_HEADS, t), lambda n, qi, kj: (0, kj[n]))
    return pl.pallas_call(
        body, name="fox_attn_bwd",
        grid_spec=pltpu.PrefetchScalarGridSpec(
            num_scalar_prefetch=2, grid=(qi.shape[0],),
            in_specs=[qblk, kblk, kblk, qblk, qcol, qcol, qcol, krow],
            out_specs=[pl.BlockSpec((s, D_HALF), lambda n, qi, kj: (0, 0)), kblk, kblk, krow],
            scratch_shapes=[pltpu.VMEM((t, D_HALF), F32), pltpu.VMEM((t, D_HALF), F32), pltpu.VMEM((N_HEADS, t), F32)]),
        out_shape=[jax.ShapeDtypeStruct((s, D_HALF), F32)] * 3 + [jax.ShapeDtypeStruct((N_HEADS, s), F32)],
        compiler_params=_params("arbitrary"),
    )(qi, kj, q, k, v, do, lse, dd, cc, cr)


def _fox_prep_bwd(u_b, h_t, dq, dk, dv, dgate, dcum, fb, qg, kg, tm=256):
    s = u_b.shape[0]
    nb = s // tm
    d = h_t.shape[0]

    def body(ub_ref, ht_ref, dq_ref, dk_ref, dv_ref, dg_ref, dc_ref, fb_ref, qg_ref, kg_ref,
             du_ref, dwb_ref, dqg_ref, dkg_ref, dfb_ref, carry):
        i = pl.program_id(0)

        @pl.when(i == 0)
        def _():
            carry[...] = jnp.zeros_like(carry)
            dwb_ref[...] = jnp.zeros_like(dwb_ref)
            dqg_ref[...] = jnp.zeros_like(dqg_ref)
            dkg_ref[...] = jnp.zeros_like(dkg_ref)
            dfb_ref[...] = jnp.zeros_like(dfb_ref)

        bd = _head_ones()
        for lo, g_ref, d_ref, dgain_ref in ((0, qg_ref, dq_ref, dqg_ref), (512, kg_ref, dk_ref, dkg_ref)):
            gain = g_ref[...]
            xh, rinv, _ = _head_rms(ub_ref[:, lo:lo + 512], gain, bd)
            dn = d_ref[...]
            dgain_ref[...] += _colsum(dn * xh)
            dxh = dn * gain
            du_ref[:, lo:lo + 512] = rinv * (dxh - xh * (_head_sum(dxh * xh, bd) * (1.0 / HEAD)))
        du_ref[:, 1024:1536] = dv_ref[...]
        du_ref[:, 1536:2048] = dg_ref[...]
        lane = lax.broadcasted_iota(jnp.int32, (1, LANES), 1)
        dc = dc_ref[...]
        dlogf = _exact_dot(dc, _tri(tm, False), ones_first=True) + carry[...]
        carry[...] += _colsum(dc)
        fl = ub_ref[:, 2048:2176] + fb_ref[...]
        dfl = jnp.where(lane < N_HEADS, dlogf * (1.0 - _sigmoid(fl)), 0.0)
        du_ref[:, 2048:2176] = dfl
        dfb_ref[...] += _colsum(dfl)
        dwb_ref[...] += jnp.dot(ht_ref[...], du_ref[...].astype(BF16), preferred_element_type=F32)

    rev = lambda w: pl.BlockSpec((tm, w), lambda i: (nb - 1 - i, 0))
    vec = lambda w: pl.BlockSpec((1, w), lambda i: (0, 0))
    return pl.pallas_call(
        body, name="fox_prep_bwd", grid=(nb,),
        in_specs=[rev(SEC), pl.BlockSpec((d, tm), lambda i: (0, nb - 1 - i))] + [rev(D_HALF)] * 4
                 + [rev(LANES), vec(LANES), vec(D_HALF), vec(D_HALF)],
        out_specs=[rev(SEC), pl.BlockSpec((d, SEC), lambda i: (0, 0)), vec(D_HALF), vec(D_HALF), vec(LANES)],
        out_shape=[jax.ShapeDtypeStruct((s, SEC), F32), jax.ShapeDtypeStruct((d, SEC), F32),
                   jax.ShapeDtypeStruct((1, D_HALF), F32), jax.ShapeDtypeStruct((1, D_HALF), F32),
                   jax.ShapeDtypeStruct((1, LANES), F32)],
        scratch_shapes=[pltpu.VMEM((1, LANES), F32)],
        compiler_params=_params("arbitrary"),
    )(u_b, h_t, dq, dk, dv, dgate, dcum, fb, qg, kg)


def _merge(y, r, k, v, gate_a, o, u_b, h, x, tgt, w_g, wa, wb, wo, fg, lw, lb, rk, tm=256):
    s, d = x.shape

    def body(y_ref, r_ref, k_ref, v_ref, ga_ref, o_ref, gb_ref, h_ref, x_ref, t_ref, wg_ref, wa_ref, wb_ref, wo_ref,
             fg_ref, lw_ref, lb_ref, rk_ref,
             dx2_ref, dy_ref, drb_ref, dkb_ref, dvb_ref, dga_ref, do_ref, dgb_ref, dug_ref,
             dwa_ref, dwb_ref, dwo_ref, dfg_ref, loss_ref, dlw_ref, dlb_ref, drk_ref):
        i = pl.program_id(0)

        @pl.when(i == 0)
        def _():
            for ref in (dwa_ref, dwb_ref, dwo_ref, dfg_ref, loss_ref, dlw_ref, dlb_ref, drk_ref):
                ref[...] = jnp.zeros_like(ref)

        bd = _head_ones()
        wa_v, wb_v, wo_v, fg_v = wa_ref[...], wb_ref[...], wo_ref[...], fg_ref[...]
        rv, kv, vv, ga, lw_v, rk_v = r_ref[...], k_ref[...], v_ref[...], ga_ref[...], lw_ref[...], rk_ref[...]
        yn, rstd, rkk, sga, pre = _rwkv_post_math(y_ref[...], rv, kv, vv, ga, lw_v, lb_ref[...], rk_v, bd)
        silu_a = ga * sga
        gb, ov = gb_ref[...], o_ref[...]
        sgb = _sigmoid(gb)
        silu_b = gb * sgb
        ma = (pre * silu_a).astype(BF16)
        mb = (ov * silu_b).astype(BF16)
        ya = jnp.dot(ma, wa_v, preferred_element_type=F32)
        yb = jnp.dot(mb, wb_v, preferred_element_type=F32)
        ug = jnp.dot(h_ref[...], wg_ref[...], preferred_element_type=F32)
        sa = _sigmoid(ug[:, 0:d])
        sb = _sigmoid(ug[:, d:2 * d])
        merged = (sa * ya + sb * yb).astype(BF16)
        x2 = x_ref[...] + jnp.dot(merged, wo_v, preferred_element_type=F32)
        r2 = lax.rsqrt(jnp.mean(x2 * x2, axis=-1, keepdims=True) + RMS_EPS)
        x2h = x2 * r2
        err = x2h * fg_v - t_ref[...]
        loss_ref[...] += _colsum(err * err)
        dyo = err * (1.0 / d)
        dfg_ref[...] += _colsum(dyo * x2h)
        dx2h = dyo * fg_v
        dx2 = r2 * (dx2h - x2h * jnp.mean(dx2h * x2h, axis=-1, keepdims=True))
        dx2_ref[...] = dx2
        dx2b = dx2.astype(BF16)
        dmerged = _dot_nt(dx2b, wo_v)
        dwo_ref[...] += _dot_tn(merged, dx2b)
        dya = dmerged * sa
        dyb = dmerged * sb
        dug_ref[:, 0:d] = dya * ya * (1.0 - sa)
        dug_ref[:, d:2 * d] = dyb * yb * (1.0 - sb)
        dyab = dya.astype(BF16)
        dybb = dyb.astype(BF16)
        dwa_ref[...] += _dot_tn(ma, dyab)
        dwb_ref[...] += _dot_tn(mb, dybb)
        dmb = _dot_nt(dybb, wb_v)
        do_ref[...] = (dmb * silu_b).astype(BF16)
        dgb_ref[...] = dmb * ov * (sgb * (1.0 + gb * (1.0 - sgb)))
        dma = _dot_nt(dyab, wa_v)
        dga_ref[...] = dma * pre * (sga * (1.0 + ga * (1.0 - sga)))
        dpre = dma * silu_a
        dlw_ref[...] += _colsum(dpre * yn)
        dlb_ref[...] += _colsum(dpre)
        dyn = dpre * lw_v
        m1 = _head_sum(dyn, bd) * (1.0 / HEAD)
        m2 = _head_sum(dyn * yn, bd) * (1.0 / HEAD)
        dy_ref[...] = rstd * (dyn - m1 - yn * m2)
        dvb_ref[...] = dpre * rkk
        drkk = _head_sum(dpre * vv, bd)
        drb_ref[...] = drkk * kv * rk_v
        dkb_ref[...] = drkk * rv * rk_v
        drk_ref[...] += _colsum(drkk * rv * kv)

    row = lambda w: pl.BlockSpec((tm, w), lambda i: (i, 0))
    full = lambda a: pl.BlockSpec(a.shape, lambda i: (0, 0))
    once = lambda a: pl.BlockSpec(a.shape, lambda i: (0, 0), pipeline_mode=pl.Buffered(1))
    half = jax.ShapeDtypeStruct((s, D_HALF), F32)
    fshape = lambda a: jax.ShapeDtypeStruct(a.shape, F32)
    return pl.pallas_call(
        body, name="merge_fwd_bwd", grid=(s // tm,),
        in_specs=[row(D_HALF)] * 6 + [pl.BlockSpec((tm, D_HALF), lambda i: (i, 3)), row(d), row(d), row(d),
                                      once(w_g), once(wa), once(wb), once(wo), full(fg), full(lw), full(lb), full(rk)],
        out_specs=[row(d)] + [row(D_HALF)] * 7 + [row(GATE_COLS), full(wa), full(wb), full(wo), full(fg), full(fg),
                                                   full(lw), full(lb), full(rk)],
        out_shape=[jax.ShapeDtypeStruct((s, d), F32)] + [half] * 5 + [jax.ShapeDtypeStruct((s, D_HALF), BF16), half,
                                                                    jax.ShapeDtypeStruct((s, GATE_COLS), F32),
                                                                    fshape(wa), fshape(wb), fshape(wo), fshape(fg),
                                                                    fshape(fg), fshape(lw), fshape(lb), fshape(rk)],
        compiler_params=_params("arbitrary"),
    )(y, r, k, v, gate_a, o, u_b, h, x, tgt, w_g, wa, wb, wo, fg, lw, lb, rk)


def _lora_weight(w_up, a_up):
    z = jnp.zeros((LORA, D_HALF), w_up.dtype)
    return jnp.concatenate([jnp.concatenate([w_up, z], axis=1), jnp.concatenate([z, a_up], axis=1)], axis=0)


def _device_grads(x, tgt, p, w_a, w_up, a_up, late_weights, fwd_exchange=None, bwd_exchange=None, tail_exchange=None):
    wl = _lora_weight(w_up, a_up)
    rk = p["r_k"].reshape(1, D_HALF)
    fb = jnp.pad(p["f_bias"], ((0, 0), (0, LANES - N_HEADS)))
    qg = jnp.tile(p["q_norm_g"], (1, N_HEADS))
    kg = jnp.tile(p["k_norm_g"], (1, N_HEADS))
    fg = p["final_norm_g"].reshape(1, D_MODEL)
    mixer = (p["shift_mu"], wl, p["w0"], p["a0"], p["k_k"], p["k_a"])

    h, u_a, r, dec, k, v, av, bv, gate_a = _rwkv_front(x, p["norm_g"], w_a, *mixer)
    y, st, arrived = _wkv_fwd(r, dec, k, av, bv, v, fwd_exchange)

    w_b, w_g, w_out_a, w_out_b, w_out = late_weights(arrived)
    u_b, q, kn, vb, cc, cr = _fox_front(h, w_b, fb, qg, kg)
    o, lse = _attn_fwd(q, kn, vb, cc, cr)

    (dx2, dy, dr_b, dk_b, dv_b, dgate_a, do, dgate_b, du_g, dwa, dwb, dwo, dfg, loss_vec, dlw, dlb, drk) = _merge(
        y, r, k, v, gate_a, o, u_b, h, x, tgt, w_g, w_out_a, w_out_b, w_out, fg, p["lnx_w"], p["lnx_b"], rk)

    dd = _attn_bwd_rowdot(q, kn, vb, do, lse, cc, cr)
    dq, dk_att, dv_att, dcr = _attn_bwd(q, kn, vb, do, lse, dd, cc, cr)
    dcum = jnp.pad(dcr.T, ((0, 0), (0, LANES - N_HEADS)))
    h_t = h.T
    du_b, dw_b, dqg, dkg, dfb = _fox_prep_bwd(u_b, h_t, dq, dk_att, dv_att, dgate_b, dcum, fb, qg, kg)
    dw_g = _matmul_tn_acc(h_t, du_g, "dw_gate")

    scan_grads, sent = _wkv_bwd(r, dec, k, av, bv, v, dy, st,
                                bwd_exchange(dw_b, dw_g, dwa, dwb, dwo) if bwd_exchange else None)
    du_a, dw_a, dmu, dwl, dw0, da0, dkkw, dkaw = _rwkv_prep_bwd(
        u_a, h_t, (*scan_grads, dr_b, dk_b, dv_b, dgate_a), *mixer)
    dw_up, da_up = dwl[:LORA, :D_HALF], dwl[LORA:, D_HALF:]
    sent_last = _run_on_sequencer(tail_exchange(dw_a, dw_up, da_up), "scatter_tail", 1) if tail_exchange else []
    grad_x, dnorm_g, _ = _inproj_bwd(du_a, du_b, du_g, w_a, w_b, w_g, x, dx2, p["norm_g"])

    grads = dict(
        norm_g=dnorm_g, w_in=(dw_a, dw_b, dw_g), shift_mu=dmu,
        w_lora_up=dw_up, w0=dw0, a_lora_up=da_up, a0=da0, k_k=dkkw, k_a=dkaw,
        r_k=drk.reshape(1, N_HEADS, HEAD), lnx_w=dlw, lnx_b=dlb, f_bias=dfb[:, :N_HEADS],
        q_norm_g=dqg.reshape(N_HEADS, HEAD).sum(axis=0, keepdims=True),
        k_norm_g=dkg.reshape(N_HEADS, HEAD).sum(axis=0, keepdims=True),
        w_out_a=dwa, w_out_b=dwb, w_out=dwo, final_norm_g=dfg.reshape(D_MODEL))
    return loss_vec, grad_x, grads, sent, sent_last


CHIP_FLIPS = ((1, 0), (0, 1), (1, 1))
ANY = pl.BlockSpec(memory_space=pl.ANY)


def _position():
    return lax.axis_index("x"), lax.axis_index("y"), lax.axis_index("c")


def _flip(v, f):
    return 1 - v if f else v


def _both(a, b):
    if a is None:
        return b
    return a if b is None else jnp.logical_and(a, b)


def _when(cond, fn):
    if cond is None:
        fn()
    else:
        pl.when(cond)(fn)


class _Moves:
    def __init__(self, send_sems, recv_sems, local_sems):
        self.send_sems, self.recv_sems, self.local_sems = send_sems, recv_sems, local_sems
        self.remote, self.local = [], []

    def send(self, src, dst, peer, landing, send_if=None, recv_if=None, first=False):
        k = len(self.remote)
        sems = dict(send_sem=self.send_sems.at[k], recv_sem=self.recv_sems.at[k], device_id=peer, device_id_type=MESH)
        out = pltpu.make_async_remote_copy(src_ref=src, dst_ref=dst, **sems)
        arrival = pltpu.make_async_remote_copy(src_ref=src, dst_ref=landing, **sems)
        self.remote.append((out, arrival, send_if, recv_if, first))

    def copy(self, src, dst, cond=None):
        cp = pltpu.make_async_copy(src, dst, self.local_sems.at[len(self.local)])
        self.local.append((cp, cond))

    def start(self, also=None):
        for cp, cond in self.local:
            _when(_both(also, cond), cp.start)
        for out, _, send_if, _, _ in self.remote:
            _when(_both(also, send_if), out.start)

    def wait_arrivals(self, also=None, first=None):
        for _, arrival, _, recv_if, is_first in self.remote:
            if first is None or first == is_first:
                _when(_both(also, recv_if), arrival.wait_recv)

    def wait_sent(self, also=None):
        for out, _, send_if, _, _ in self.remote:
            _when(_both(also, send_if), out.wait_send)
        for cp, cond in self.local:
            _when(_both(also, cond), cp.wait)

    def wait(self, also=None):
        self.wait_arrivals(also)
        self.wait_sent(also)


class _Exchange:
    def __init__(self, operands, out_shapes, n_remote, n_local, build, relays=None, in_place=(), n_staging=0):
        self.operands, self.out_shapes = list(operands), list(out_shapes)
        self.n_remote, self.n_local, self.build = n_remote, n_local, build
        self.relays, self.in_place = relays, in_place
        self.n_staging = n_staging

    def scratch(self):
        return [pltpu.SemaphoreType.DMA((self.n_remote,)), pltpu.SemaphoreType.DMA((self.n_remote,)),
                pltpu.SemaphoreType.DMA((max(self.n_local, 1),))]

    def moves(self, in_refs, out_refs, sems):
        mv = _Moves(*sems)
        self.build(mv, in_refs, out_refs)
        return mv


def _run_on_sequencer(exchange, name, collective_id):
    ins = [jax.new_ref(a, memory_space=pltpu.MemorySpace.HBM) for a in exchange.operands]
    outs = [ins[i] if i in exchange.in_place else jax.empty_ref(s, memory_space=pltpu.MemorySpace.HBM)
            for i, s in enumerate(exchange.out_shapes)]
    forward, to_sibling = exchange.relays or (None, None)
    relay_scratch = [pltpu.SemaphoreType.DMA((stage[0],)) for stage in (forward, to_sibling) if stage for _ in range(2)]

    def launch(*sems):
        x, y, c = _position()
        peers = [(_flip(x, fx), _flip(y, fy), c) for fx, fy in CHIP_FLIPS] + ([(x, y, 1 - c)] if to_sibling else [])
        barrier = pltpu.get_barrier_semaphore()
        for peer in peers:
            pl.semaphore_signal(barrier, inc=1, device_id=peer, device_id_type=MESH)
        pl.semaphore_wait(barrier, len(peers))
        moves = exchange.moves(ins, outs, sems[:3])
        moves.start()
        later = []
        if forward:
            onward = _Moves(sems[3], sems[4], None)
            forward[1](onward, ins, outs)
            moves.wait_arrivals(first=True)
            onward.start()
            moves.wait_arrivals(first=False)
            onward.wait_arrivals()
            later.append(onward)
        else:
            moves.wait_arrivals()
        if to_sibling:
            passed = _Moves(*sems[-2:], None)
            to_sibling[1](passed, ins, outs)
            passed.start()
            passed.wait_arrivals()
            later.append(passed)
        for mv in later + [moves]:
            mv.wait_sent()

    pl.kernel(launch, mesh=plsc.ScalarSubcoreMesh(axis_name="sequencer", num_cores=1), name=name,
              scratch_types=tuple(exchange.scratch() + relay_scratch),
              compiler_params=pltpu.CompilerParams(collective_id=collective_id))()
    return [o[...] for o in outs[:len(outs) - exchange.n_staging]]


def _row_major_copy(a, name):
    r, c = a.shape
    tr = _row_tile(r)

    def body(a_ref, o_ref):
        o_ref[...] = a_ref[...]

    blk = pl.BlockSpec((tr, c), lambda i: (i, 0))
    return pl.pallas_call(body, name=name, grid=(r // tr,), in_specs=[blk], out_specs=blk,
                          out_shape=jax.ShapeDtypeStruct(a.shape, a.dtype), compiler_params=_params("parallel"))(a)


def _is_chip(x, y, chip):
    return jnp.logical_and(x == chip // 2, y == chip % 2)


def _gather_exchange(from_chip, from_all, split=()):
    n1, n2 = len(from_chip), len(from_all)
    near = CHIP_FLIPS[:2]

    def quarters(t, c, first, count=1):
        n = from_chip[t][1].shape[0] // 4
        return pl.ds((2 * c + first) * n, count * n)

    def build(mv, ins, outs):
        x, y, c = _position()
        me = 2 * x + y
        for t, (chip, _) in enumerate(from_chip):
            if t not in split:
                mv.copy(ins[t], outs[t], cond=_is_chip(x, y, chip))
        for t in range(n2):
            mv.copy(ins[n1 + t], outs[n1 + t].at[me])
        for t in split:
            for first in (True, False):
                for f, (fx, fy) in enumerate(near):
                    px, py = _flip(x, fx), _flip(y, fy)
                    part = quarters(t, c, f if first else 1 - f)
                    mv.send(ins[t].at[part], outs[t].at[part], (px, py, c), landing=outs[t].at[part], first=first,
                            send_if=_is_chip(x, y, from_chip[t][0]), recv_if=_is_chip(px, py, from_chip[t][0]))
        for fx, fy in CHIP_FLIPS:
            px, py = _flip(x, fx), _flip(y, fy)
            peer = (px, py, c)
            for t, (chip, _) in enumerate(from_chip):
                if t not in split:
                    mv.send(ins[t], outs[t], peer, landing=outs[t],
                            send_if=_is_chip(x, y, chip), recv_if=_is_chip(px, py, chip))
            for t in range(n2):
                mv.send(ins[n1 + t], outs[n1 + t].at[me], peer, landing=outs[n1 + t].at[2 * px + py])

    def forward(mv, ins, outs):
        x, y, c = _position()
        for t in split:
            chip = from_chip[t][0]
            for f, (fx, fy) in enumerate(near):
                gx, gy = near[1 - f]
                part = quarters(t, c, f)
                mv.send(outs[t].at[part], outs[t].at[part], (_flip(x, gx), _flip(y, gy), c), landing=outs[t].at[part],
                        send_if=_is_chip(_flip(x, fx), _flip(y, fy), chip), recv_if=_is_chip(1 - x, 1 - y, chip))

    def to_sibling(mv, ins, outs):
        x, y, c = _position()
        for t in split:
            came = jnp.logical_not(_is_chip(x, y, from_chip[t][0]))
            mv.send(outs[t].at[quarters(t, c, 0, 2)], outs[t].at[quarters(t, c, 0, 2)], (x, y, 1 - c),
                    landing=outs[t].at[quarters(t, 1 - c, 0, 2)], send_if=came, recv_if=came)

    arrays = [a for _, a in from_chip] + list(from_all)
    shapes = [jax.ShapeDtypeStruct(a.shape, a.dtype) for _, a in from_chip]
    shapes += [jax.ShapeDtypeStruct((N_CHIPS,) + a.shape, a.dtype) for a in from_all]
    n_remote = len(CHIP_FLIPS) * (n1 - len(split) + n2) + 2 * len(near) * len(split)
    relays = ((len(near) * len(split), forward), (len(split), to_sibling)) if split else None
    return _Exchange(arrays, shapes, n_remote, n1 + n2, build, relays, in_place=split)


def _scatter_exchange(to_chip, to_all, via_neighbours=False):
    n1, n2 = len(to_chip), len(to_all)
    near = CHIP_FLIPS[:2]
    direct = near if via_neighbours else CHIP_FLIPS

    def half(t, g):
        n = to_chip[t][1].shape[0] // 2
        return pl.ds(g * n, n)

    def build(mv, ins, outs):
        x, y, c = _position()
        if via_neighbours:
            for t, (chip, _) in enumerate(to_chip):
                for g, (gx, gy) in enumerate(near):
                    ox, oy = near[1 - g]
                    mv.send(ins[t].at[half(t, g)], outs[n1 + n2 + t], (_flip(x, gx), _flip(y, gy), c),
                            landing=outs[n1 + n2 + t], first=True, send_if=_is_chip(1 - x, 1 - y, chip),
                            recv_if=_is_chip(_flip(x, ox), _flip(y, oy), chip))
        for f, (fx, fy) in enumerate(CHIP_FLIPS):
            px, py = _flip(x, fx), _flip(y, fy)
            peer = (px, py, c)
            if (fx, fy) in direct:
                for t, (chip, _) in enumerate(to_chip):
                    mv.send(ins[t], outs[t].at[f], peer, landing=outs[t].at[f],
                            send_if=_is_chip(px, py, chip), recv_if=_is_chip(x, y, chip))
            for t in range(n2):
                mv.send(ins[n1 + t].at[2 * px + py], outs[n1 + t].at[f], peer, landing=outs[n1 + t].at[f])

    def forward(mv, ins, outs):
        x, y, c = _position()
        for t, (chip, _) in enumerate(to_chip):
            for g in range(len(near)):
                ox, oy = near[1 - g]
                far_slot = outs[t].at[len(near)].at[half(t, g)]
                mv.send(outs[n1 + n2 + t], far_slot, (_flip(x, ox), _flip(y, oy), c), landing=far_slot,
                        send_if=_is_chip(_flip(x, ox), _flip(y, oy), chip), recv_if=_is_chip(x, y, chip))

    arrays = [a for _, a in to_chip] + list(to_all)
    shapes = [jax.ShapeDtypeStruct((len(CHIP_FLIPS),) + a.shape, a.dtype) for _, a in to_chip]
    shapes += [jax.ShapeDtypeStruct((len(CHIP_FLIPS),) + a.shape[1:], a.dtype) for a in to_all]
    if not via_neighbours:
        return _Exchange(arrays, shapes, len(CHIP_FLIPS) * (n1 + n2), 0, build)
    shapes += [jax.ShapeDtypeStruct((a.shape[0] // 2, a.shape[1]), a.dtype) for _, a in to_chip]
    return _Exchange(arrays, shapes, 2 * len(near) * n1 + len(CHIP_FLIPS) * n2, 0, build,
                     relays=((len(near) * n1, forward), None), n_staging=n1)


def _swap_sibling(tensors, name):
    n = len(tensors)

    def body(*refs):
        ins, outs = refs[:n], refs[n:2 * n]
        send_sems, recv_sems = refs[2 * n:]
        x, y, c = _position()
        copies = [pltpu.make_async_remote_copy(
            src_ref=ins[t], dst_ref=outs[t], send_sem=send_sems.at[t], recv_sem=recv_sems.at[t],
            device_id=(x, y, 1 - c), device_id_type=MESH) for t in range(n)]
        for cp in copies:
            cp.start()
        for cp in copies:
            cp.wait_recv()
        for cp in copies:
            cp.wait_send()

    return pl.pallas_call(
        body, name=name, in_specs=[ANY] * n, out_specs=[ANY] * n,
        out_shape=[jax.ShapeDtypeStruct(a.shape, a.dtype) for a in tensors],
        scratch_shapes=[pltpu.SemaphoreType.DMA((n,)), pltpu.SemaphoreType.DMA((n,))],
        compiler_params=pltpu.CompilerParams(has_side_effects=True),
    )(*tensors)


def _pair_halves(g):
    r, cols = g.shape
    half = r // 2

    def body(g_ref, o_ref, mine, theirs, send_sem, recv_sem, local_sem):
        x, y, c = _position()
        away = pltpu.make_async_remote_copy(
            src_ref=g_ref.at[pl.ds((1 - c) * half, half)], dst_ref=theirs, send_sem=send_sem, recv_sem=recv_sem,
            device_id=(x, y, 1 - c), device_id_type=MESH)
        kept = pltpu.make_async_copy(g_ref.at[pl.ds(c * half, half)], mine, local_sem)
        away.start()
        kept.start()
        kept.wait()
        away.wait_recv()
        o_ref[...] = (mine[...] + theirs[...]).astype(BF16)
        away.wait_send()

    return pl.pallas_call(
        body, name="pair_halves", in_specs=[ANY], out_specs=pl.BlockSpec(memory_space=pltpu.VMEM),
        out_shape=jax.ShapeDtypeStruct((half, cols), BF16),
        scratch_shapes=[pltpu.VMEM((half, cols), F32), pltpu.VMEM((half, cols), F32),
                        pltpu.SemaphoreType.DMA(()), pltpu.SemaphoreType.DMA(()), pltpu.SemaphoreType.DMA(())],
        compiler_params=pltpu.CompilerParams(has_side_effects=True, vmem_limit_bytes=VMEM_LIMIT),
    )(g)


def _allreduce_small(slab):
    stages = 3

    def body(x_ref, o_ref, buf, send_sems, recv_sems):
        x, y, c = _position()
        peers = ((1 - x, y, c), (x, 1 - y, c), (x, y, 1 - c))
        o_ref[...] = x_ref[...]
        for k, peer in enumerate(peers):
            cp = pltpu.make_async_remote_copy(src_ref=o_ref, dst_ref=buf.at[k], send_sem=send_sems.at[k],
                                              recv_sem=recv_sems.at[k], device_id=peer, device_id_type=MESH)
            cp.start()
            cp.wait()
            o_ref[...] = o_ref[...] + buf[k]

    return pl.pallas_call(
        body, name="allreduce_small",
        in_specs=[pl.BlockSpec(memory_space=pltpu.VMEM)], out_specs=pl.BlockSpec(memory_space=pltpu.VMEM),
        out_shape=jax.ShapeDtypeStruct(slab.shape, slab.dtype),
        scratch_shapes=[pltpu.VMEM((stages,) + slab.shape, slab.dtype),
                        pltpu.SemaphoreType.DMA((stages,)), pltpu.SemaphoreType.DMA((stages,))],
        compiler_params=pltpu.CompilerParams(has_side_effects=True),
    )(slab)


def _row_tile(r):
    return min(r, 256)


def _sum4(stack, recv, me):
    _, r, c = stack.shape
    tr = _row_tile(r)

    def body(me_ref, own_ref, recv_ref, o_ref):
        o_ref[...] = (((own_ref[...] + recv_ref[0].astype(F32)) + recv_ref[1].astype(F32))
                      + recv_ref[2].astype(F32))

    return pl.pallas_call(
        body, name="sum_partials",
        grid_spec=pltpu.PrefetchScalarGridSpec(
            num_scalar_prefetch=1, grid=(r // tr,),
            in_specs=[pl.BlockSpec((None, tr, c), lambda i, me_ref: (me_ref[0], i, 0)),
                      pl.BlockSpec((len(CHIP_FLIPS), tr, c), lambda i, me_ref: (0, i, 0))],
            out_specs=pl.BlockSpec((tr, c), lambda i, me_ref: (i, 0))),
        out_shape=jax.ShapeDtypeStruct((r, c), F32), compiler_params=_params("parallel"),
    )(me, stack, recv)


def _sum_block(own, recv):
    r, c = own.shape
    tr = _row_tile(r)

    def body(own_ref, recv_ref, o_ref):
        o_ref[...] = (((own_ref[...] + recv_ref[0].astype(F32)) + recv_ref[1].astype(F32))
                      + recv_ref[2].astype(F32))

    return pl.pallas_call(
        body, name="sum_block", grid=(r // tr,),
        in_specs=[pl.BlockSpec((tr, c), lambda i: (i, 0)), pl.BlockSpec((len(CHIP_FLIPS), tr, c), lambda i: (0, i, 0))],
        out_specs=pl.BlockSpec((tr, c), lambda i: (i, 0)),
        out_shape=jax.ShapeDtypeStruct((r, c), F32), compiler_params=_params("parallel"),
    )(own, recv)


def _sum_half(own, recv, core):
    r, c = own.shape
    tr = _row_tile(r // 2)
    per_half = r // 2 // tr

    def body(core_ref, own_ref, recv_ref, o_ref):
        mine = pl.program_id(0) // per_half == core_ref[0]

        @pl.when(mine)
        def _():
            o_ref[...] = (((own_ref[...] + recv_ref[0].astype(F32)) + recv_ref[1].astype(F32))
                          + recv_ref[2].astype(F32))

        @pl.when(jnp.logical_not(mine))
        def _():
            o_ref[...] = own_ref[...]

    return pl.pallas_call(
        body, name="sum_half",
        grid_spec=pltpu.PrefetchScalarGridSpec(
            num_scalar_prefetch=1, grid=(r // tr,),
            in_specs=[pl.BlockSpec((tr, c), lambda i, core: (i, 0)),
                      pl.BlockSpec((len(CHIP_FLIPS), tr, c), lambda i, core: (0, i % per_half, 0))],
            out_specs=pl.BlockSpec((tr, c), lambda i, core: (i, 0))),
        out_shape=jax.ShapeDtypeStruct((r, c), F32), compiler_params=_params("parallel"),
    )(core, own, recv)


def _adamw_math(w, g, m, v):
    m = ADAM_B1 * m + (1.0 - ADAM_B1) * g
    v = ADAM_B2 * v + (1.0 - ADAM_B2) * (g * g)
    m_hat = m / (1.0 - ADAM_B1 ** ADAM_STEP)
    v_hat = v / (1.0 - ADAM_B2 ** ADAM_STEP)
    delta = -ADAM_LR * (m_hat / (jnp.sqrt(v_hat) + ADAM_EPS) + ADAM_WD * w)
    return delta, m, v


def _adamw(w, m, v, g_parts, name):
    r, c = w.shape
    tr = _row_tile(r)
    n = len(g_parts)

    def body(*refs):
        w_ref, m_ref, v_ref = refs[:3]
        g_refs = refs[3:3 + n]
        g_out, d_out, m_out, v_out = refs[3 + n:]
        g = g_refs[0][...]
        for ref in g_refs[1:]:
            g = g + ref[...]
        g_out[...] = g
        d_out[...], m_out[...], v_out[...] = _adamw_math(w_ref[...], g, m_ref[...], v_ref[...])

    blk = pl.BlockSpec((tr, c), lambda i: (i, 0))
    return pl.pallas_call(
        body, name=name, grid=(r // tr,), in_specs=[blk] * (3 + n), out_specs=[blk] * 4,
        out_shape=[jax.ShapeDtypeStruct((r, c), F32)] * 4, compiler_params=_params("parallel"),
    )(w, m, v, *g_parts)


def _adamw_small(total, w, m, v):
    sizes = [w[n].size for n in SMALL]
    flat = lambda d: [d[n].reshape(1, -1) for n in SMALL]
    k = len(SMALL)

    def body(*refs):
        total_ref, w_refs, m_refs, v_refs = refs[0], refs[1:1 + k], refs[1 + k:1 + 2 * k], refs[1 + 2 * k:1 + 3 * k]
        outs = refs[1 + 3 * k:]
        for i, size in enumerate(sizes):
            g = total_ref[i:i + 1, 0:size]
            outs[i][...] = g
            outs[k + i][...], outs[2 * k + i][...], outs[3 * k + i][...] = _adamw_math(
                w_refs[i][...], g, m_refs[i][...], v_refs[i][...])

    res = pl.pallas_call(
        body, name="adamw_small", out_shape=[jax.ShapeDtypeStruct((1, size), F32) for size in sizes] * 4,
        compiler_params=_params(),
    )(total, *flat(w), *flat(m), *flat(v))
    return [{n: res[j * k + i].reshape(w[n].shape) for i, n in enumerate(SMALL)} for j in range(4)]


SHARDED = ("w_in", "w_lora_up", "a_lora_up", "w_out_a", "w_out_b", "w_out")
ROW_SHARDED = ("w_out",)
SMALL = ("norm_g", "shift_mu", "w0", "a0", "k_k", "k_a", "r_k", "lnx_w", "lnx_b", "f_bias", "q_norm_g", "k_norm_g",
         "final_norm_g")
WEIGHTS = ("norm_g", "w_in", "shift_mu", "w_lora_up", "w0", "a_lora_up", "a0", "k_k", "k_a", "r_k", "lnx_w", "lnx_b",
           "f_bias", "q_norm_g", "k_norm_g", "w_out_a", "w_out_b", "w_out", "final_norm_g")
SLAB_ROWS = 16
SLAB_COLS = SEC


def _to_slab(named, extra=None):
    rows = [jnp.pad(named[n].reshape(1, -1), ((0, 0), (0, SLAB_COLS - named[n].size))) for n in SMALL]
    if extra is not None:
        rows.append(jnp.pad(extra.reshape(1, -1), ((0, 0), (0, SLAB_COLS - extra.size))))
    rows.append(jnp.zeros((SLAB_ROWS - len(rows), SLAB_COLS), F32))
    return jnp.concatenate(rows, axis=0)


def _by_chip(g, name):
    if name in ROW_SHARDED:
        return g.reshape(N_CHIPS, g.shape[0] // N_CHIPS, g.shape[1])
    r, c = g.shape
    return g.reshape(r, N_CHIPS, c // N_CHIPS).transpose(1, 0, 2)


def _from_chips(stack, name):
    if name in ROW_SHARDED:
        return stack.reshape(-1, stack.shape[2])
    _, r, c = stack.shape
    return stack.transpose(1, 0, 2).reshape(r, N_CHIPS * c)


def kernel(x, norm_g, w_in, shift_mu, w_lora_up, w0, a_lora_up, a0, k_k, k_a, r_k, lnx_w, lnx_b, f_bias, q_norm_g, k_norm_g, w_out_a, w_out_b, w_out, final_norm_g, loss_target, m_norm_g, m_w_in, m_shift_mu, m_w_lora_up, m_w0, m_a_lora_up, m_a0, m_k_k, m_k_a, m_r_k, m_lnx_w, m_lnx_b, m_f_bias, m_q_norm_g, m_k_norm_g, m_w_out_a, m_w_out_b, m_w_out, m_final_norm_g, v_norm_g, v_w_in, v_shift_mu, v_w_lora_up, v_w0, v_a_lora_up, v_a0, v_k_k, v_k_a, v_r_k, v_lnx_w, v_lnx_b, v_f_bias, v_q_norm_g, v_k_norm_g, v_w_out_a, v_w_out_b, v_w_out, v_final_norm_g):
    w = dict(norm_g=norm_g, w_in=w_in, shift_mu=shift_mu, w_lora_up=w_lora_up, w0=w0, a_lora_up=a_lora_up, a0=a0,
             k_k=k_k, k_a=k_a, r_k=r_k, lnx_w=lnx_w, lnx_b=lnx_b, f_bias=f_bias, q_norm_g=q_norm_g,
             k_norm_g=k_norm_g, w_out_a=w_out_a, w_out_b=w_out_b, w_out=w_out, final_norm_g=final_norm_g)
    m = dict(norm_g=m_norm_g, w_in=m_w_in, shift_mu=m_shift_mu, w_lora_up=m_w_lora_up, w0=m_w0,
             a_lora_up=m_a_lora_up, a0=m_a0, k_k=m_k_k, k_a=m_k_a, r_k=m_r_k, lnx_w=m_lnx_w, lnx_b=m_lnx_b,
             f_bias=m_f_bias, q_norm_g=m_q_norm_g, k_norm_g=m_k_norm_g, w_out_a=m_w_out_a, w_out_b=m_w_out_b,
             w_out=m_w_out, final_norm_g=m_final_norm_g)
    v = dict(norm_g=v_norm_g, w_in=v_w_in, shift_mu=v_shift_mu, w_lora_up=v_w_lora_up, w0=v_w0,
             a_lora_up=v_a_lora_up, a0=v_a0, k_k=v_k_k, k_a=v_k_a, r_k=v_r_k, lnx_w=v_lnx_w, lnx_b=v_lnx_b,
             f_bias=v_f_bias, q_norm_g=v_q_norm_g, k_norm_g=v_k_norm_g, w_out_a=v_w_out_a, w_out_b=v_w_out_b,
             w_out=v_w_out, final_norm_g=v_final_norm_g)
    shapes = {n: w[n].shape for n in WEIGHTS}

    shard = {n: w[n][0].astype(BF16) for n in SHARDED}
    late = ("w_out_a", "w_out_b", "w_out")
    loras = ("w_lora_up", "a_lora_up")
    w_in_head, w_in_tail = shard["w_in"][:, :A_TAIL], shard["w_in"][:, A_TAIL:]
    shard0, shard1_head, up_stack, aup_stack = _run_on_sequencer(_gather_exchange(
        [(0, shard["w_in"]), (1, w_in_head)], [shard[n] for n in loras], split=(0, 1)), "gather_early", 2)
    moments = (_row_major_copy(m["w_in"][0], "m_w_in_rows"), _row_major_copy(v["w_in"][0], "v_w_in_rows"))
    shard0, moments = lax.optimization_barrier((shard0, moments))
    w_a = jnp.concatenate([shard0, shard1_head], axis=1)

    def late_weights(arrived):
        shard1_tail, shard2, shard3 = arrived[:3]
        w_b = jnp.concatenate([shard1_tail, shard2[:, :B_TAIL], jnp.zeros((D_MODEL, SEC - FOX_REAL), BF16)], axis=1)
        w_g = jnp.concatenate([shard2[:, B_TAIL:], shard3], axis=1)
        return (w_b, w_g, *[_from_chips(s, n) for n, s in zip(late, arrived[3:])])

    own = {}
    cut = {"block0": lambda: own["dw_a"][:, :SHARD_COLS], "head1": lambda: own["dw_a"][:, SHARD_COLS:],
           "tail1": lambda: own["dw_b"][:, :B_HEAD],
           "block2": lambda: jnp.concatenate([own["dw_b"][:, B_HEAD:FOX_REAL], own["dw_g"][:, :G_HEAD]], axis=1),
           "block3": lambda: own["dw_g"][:, G_HEAD:]}

    def bwd_exchange(dw_b, dw_g, dwa, dwb, dwo):
        own.update(dw_b=dw_b, dw_g=dw_g)
        own.update({n: _by_chip(g, n) for n, g in zip(late, (dwa, dwb, dwo))})
        return _scatter_exchange([(1, cut["tail1"]().astype(BF16)), (2, cut["block2"]().astype(BF16)),
                                  (3, cut["block3"]().astype(BF16))], [own[n].astype(BF16) for n in late])

    def tail_exchange(dw_a, dw_up, da_up):
        own.update(dw_a=dw_a)
        own.update({n: _by_chip(g, n) for n, g in zip(loras, (dw_up, da_up))})
        pair = _pair_halves(dw_a)
        return _scatter_exchange([(0, pair[:, :SHARD_COLS]), (1, pair[:, SHARD_COLS:])],
                                 [own[n].astype(BF16) for n in loras], via_neighbours=True)

    small = {n: w[n] for n in SMALL}
    loss_vec, grad_x, grads, sent, sent_last = _device_grads(
        x[0], loss_target[0], small, w_a, _from_chips(up_stack, "w_lora_up"), _from_chips(aup_stack, "a_lora_up"),
        late_weights, _gather_exchange([(1, w_in_tail), (2, shard["w_in"]), (3, shard["w_in"])], [shard[n] for n in late]),
        bwd_exchange, tail_exchange)

    total = _allreduce_small(_to_slab(grads, extra=loss_vec))
    loss = (0.5 / D_MODEL) * jnp.sum(total[len(SMALL)])
    out_g, out_d, out_m, out_v = _adamw_small(total, w, m, v)

    xpos, ypos, cpos = _position()
    me = (2 * xpos + ypos).astype(jnp.int32).reshape(1)
    core = cpos.astype(jnp.int32).reshape(1)
    core_sum = {n: _sum4(own[n], r, me) for n, r in zip(late, sent[3:])}
    theirs = dict(zip(late, _swap_sibling([core_sum[n] for n in late], "swap_sibling_early")))

    def update(n):
        m_n, v_n = moments if n == "w_in" else (m[n][0], v[n][0])
        g, d, m2, v2 = _adamw(w[n][0], m_n, v_n, [core_sum[n], theirs[n]], "adamw_" + n)
        out_g[n], out_d[n], out_m[n], out_v[n] = (a.reshape(shapes[n]) for a in (g, d, m2, v2))

    for n in late:
        update(n)
    done = (out_d["norm_g"], [out_d[n] for n in late])
    sent_last, (out_d["norm_g"], new_d) = lax.optimization_barrier((sent_last, done))
    out_d.update(zip(late, new_d))
    core_sum["w_in"] = lax.switch(me[0], [
        lambda: _sum_half(cut["block0"](), sent_last[0], core),
        lambda: jnp.concatenate([_sum_half(cut["head1"](), sent_last[1], core), _sum_block(cut["tail1"](), sent[0])],
                                axis=1),
        lambda: _sum_block(cut["block2"](), sent[1]),
        lambda: _sum_block(cut["block3"](), sent[2])])
    core_sum.update({n: _sum4(own[n], r, me) for n, r in zip(loras, sent_last[2:])})
    rest = ("w_in",) + loras
    theirs.update(zip(rest, _swap_sibling([core_sum[n] for n in rest], "swap_sibling")))
    for n in rest:
        update(n)

    return (loss, grad_x.reshape(x.shape), *[out_g[n] for n in WEIGHTS], *[out_d[n] for n in WEIGHTS],
            *[out_m[n] for n in WEIGHTS], *[out_v[n] for n in WEIGHTS])
```

```python
import functools
import math

import jax
import jax.numpy as jnp
from jax import lax
from jax.experimental import pallas as pl
from jax.experimental.pallas import tpu as pltpu
from jax.experimental.pallas import tpu_sc as plsc

F32 = jnp.float32
BF16 = jnp.bfloat16

D_MODEL = 1024
D_HALF = 512
HEAD = 64
N_HEADS = 8
LORA = 64
RWKV_COLS = 2176
FOX_REAL = 2056
SEC = 2176
GATE_COLS = 2048
IN_COLS = 6280
N_CHIPS = 4
SHARD_COLS = IN_COLS // N_CHIPS
A_TAIL = RWKV_COLS - SHARD_COLS
B_HEAD = SHARD_COLS - A_TAIL
B_TAIL = FOX_REAL - B_HEAD
G_HEAD = SHARD_COLS - B_TAIL
RMS_EPS = 1e-6
LNX_EPS = 64e-5
ATT_SCALE = HEAD ** -0.5
NEG = -1e30

ADAM_LR = 0.001
ADAM_B1 = 0.9
ADAM_B2 = 0.999
ADAM_EPS = 1e-08
ADAM_WD = 0.01
ADAM_STEP = 10

LANES = 128
SUBLANES = 8
VMEM_LIMIT = 56 * 1024 * 1024
MESH = pl.DeviceIdType.MESH


def _params(*sem):
    return pltpu.CompilerParams(dimension_semantics=sem if sem else None, vmem_limit_bytes=VMEM_LIMIT)


def _sigmoid(x):
    return 1.0 / (1.0 + jnp.exp(-x))


def _log_sigmoid(x):
    return jnp.minimum(x, 0.0) - jnp.log(1.0 + jnp.exp(-jnp.abs(x)))


def _head_ones():
    r = lax.broadcasted_iota(jnp.int32, (LANES, LANES), 0) >> 6
    c = lax.broadcasted_iota(jnp.int32, (LANES, LANES), 1) >> 6
    return (r == c).astype(BF16)


def _split3(x):
    hi = x.astype(BF16)
    r1 = x - hi.astype(F32)
    mid = r1.astype(BF16)
    lo = (r1 - mid.astype(F32)).astype(BF16)
    return hi, mid, lo


def _exact_dot(x, ones_bf16, ones_first=False):
    out = None
    for piece in _split3(x):
        if ones_first:
            t = jnp.dot(ones_bf16, piece, preferred_element_type=F32)
        else:
            t = jnp.dot(piece, ones_bf16, preferred_element_type=F32)
        out = t if out is None else out + t
    return out


def _head_sum(x, bd):
    n = x.shape[1] // LANES
    parts = [_exact_dot(x[:, i * LANES:(i + 1) * LANES], bd) for i in range(n)]
    return parts[0] if n == 1 else jnp.concatenate(parts, axis=1)


def _dot_nt(a, b):
    return lax.dot_general(a, b, (((1,), (1,)), ((), ())), preferred_element_type=F32)


def _dot_tn(a, b):
    return lax.dot_general(a, b, (((0,), (0,)), ((), ())), preferred_element_type=F32)


def _colsum(x):
    return jnp.sum(x, axis=0, keepdims=True)


def _matmul_tn_acc(at, b, name, tk=512):
    m, k = at.shape
    n = b.shape[1]

    def body(a_ref, b_ref, o_ref):
        j = pl.program_id(0)

        @pl.when(j == 0)
        def _():
            o_ref[...] = jnp.zeros_like(o_ref)

        o_ref[...] += jnp.dot(a_ref[...], b_ref[...].astype(BF16), preferred_element_type=F32)

    return pl.pallas_call(
        body, name=name, grid=(k // tk,),
        in_specs=[pl.BlockSpec((m, tk), lambda j: (0, j)), pl.BlockSpec((tk, n), lambda j: (j, 0))],
        out_specs=pl.BlockSpec((m, n), lambda j: (0, 0)),
        out_shape=jax.ShapeDtypeStruct((m, n), F32), compiler_params=_params("arbitrary"),
    )(at, b)


def _inproj_bwd(du_a, du_b, du_g, w_a, w_b, w_g, x, dx2, g, exchange=None, tm=256):
    s, d = x.shape
    nb = s // tm

    def body(*refs):
        ((da_ref, db_ref, dg_ref, wa_ref, wb_ref, wg_ref, x_ref, dx2_ref, g_ref), (gx_ref, gg_ref), _,
         moves) = _split_refs(refs, 9, 2, exchange)
        i = pl.program_id(0)
        if moves:
            moves.start(also=(i == 0))

        @pl.when(i == 0)
        def _():
            gg_ref[...] = jnp.zeros_like(gg_ref)

        dh = _dot_nt(da_ref[...].astype(BF16), wa_ref[...])
        dh += _dot_nt(db_ref[...].astype(BF16), wb_ref[...])
        dh += _dot_nt(dg_ref[...].astype(BF16), wg_ref[...])
        xv = x_ref[...]
        r = lax.rsqrt(jnp.mean(xv * xv, axis=-1, keepdims=True) + RMS_EPS)
        xh = xv * r
        gg_ref[...] += _colsum(dh * xh)
        dxh = dh * g_ref[...]
        gx_ref[...] = dx2_ref[...] + r * (dxh - xh * jnp.mean(dxh * xh, axis=-1, keepdims=True))
        if moves:
            moves.wait(also=(i == nb - 1))

    row = lambda w: pl.BlockSpec((tm, w), lambda i: (i, 0))
    full = lambda a: pl.BlockSpec(a.shape, lambda i: (0, 0))
    ex_in = exchange.operands if exchange else []
    ex_out = exchange.out_shapes if exchange else []
    res = pl.pallas_call(
        body, name="inproj_bwd", grid=(nb,),
        in_specs=[row(SEC), row(SEC), row(GATE_COLS), full(w_a), full(w_b), full(w_g), row(d), row(d), full(g)]
                 + [ANY] * len(ex_in),
        out_specs=[row(d), pl.BlockSpec((1, d), lambda i: (0, 0))] + [ANY] * len(ex_out),
        out_shape=[jax.ShapeDtypeStruct((s, d), F32), jax.ShapeDtypeStruct((1, d), F32)] + ex_out,
        scratch_shapes=exchange.scratch() if exchange else [],
        compiler_params=_params("arbitrary"),
    )(du_a, du_b, du_g, w_a, w_b, w_g, x, dx2, g, *ex_in)
    return res[0], res[1], list(res[2:])


def _rwkv_elementwise(ua, prev_row, first, mu, wl, w0, a0, kkw, kaw, bd):
    tm = ua.shape[0]
    rows = lax.broadcasted_iota(jnp.int32, (tm, 1), 0)
    prev = jnp.where(first, jnp.zeros_like(prev_row), prev_row)
    shifted = jnp.where(rows == 0, prev, pltpu.roll(ua, 1, 0))
    delta = shifted - ua
    us = ua + delta * mu
    r = us[:, 0:512]
    k0 = us[:, 512:1024]
    v = us[:, 1024:1536]
    lo = us[:, 1536:1664]
    gate = us[:, 1664:2176]
    lane = lax.broadcasted_iota(jnp.int32, (1, LANES), 1)
    th = jnp.tanh(lo)
    lin = jnp.where(lane < LORA, th, lo)
    ll = jnp.dot(lin.astype(BF16), wl, preferred_element_type=F32)
    sz = _sigmoid(w0 + ll[:, :512])
    e = sz * math.exp(-0.5)
    dec = jnp.exp(-e)
    a = _sigmoid(a0 + ll[:, 512:])
    kk0 = k0 * kkw
    ss = _head_sum(kk0 * kk0, bd)
    nrm = jnp.maximum(jnp.sqrt(ss), 1e-12)
    kk = kk0 / nrm
    k = k0 * (1.0 + (a - 1.0) * kaw)
    return dict(delta=delta, us=us, r=r, k0=k0, v=v, lo=lo, gate=gate, th=th, lin=lin, sz=sz, e=e, dec=dec,
                a=a, kk0=kk0, ss=ss, nrm=nrm, kk=kk, k=k)


def _rwkv_front(x, g, w_a, mu, wl, w0, a0, kkw, kaw, tm=256):
    s, d = x.shape

    def body(x_ref, g_ref, wa_ref, mu_ref, wl_ref, w0_ref, a0_ref, kkw_ref, kaw_ref,
             h_ref, ua_ref, r_ref, w_ref, k_ref, v_ref, a_ref, b_ref, gate_ref, last_row):
        i = pl.program_id(0)
        xv = x_ref[...]
        h = (xv * lax.rsqrt(jnp.mean(xv * xv, axis=-1, keepdims=True) + RMS_EPS) * g_ref[...]).astype(BF16)
        h_ref[...] = h
        ua = jnp.dot(h, wa_ref[...], preferred_element_type=F32)
        ua_ref[...] = ua

        @pl.when(i == 0)
        def _():
            last_row[...] = jnp.zeros_like(last_row)

        f = _rwkv_elementwise(ua, last_row[...], i == 0, mu_ref[...], wl_ref[...], w0_ref[...],
                              a0_ref[...], kkw_ref[...], kaw_ref[...], _head_ones())
        last_row[...] = ua[tm - 1:tm, :]
        r_ref[...] = f["r"]
        w_ref[...] = f["dec"]
        k_ref[...] = f["k"]
        v_ref[...] = f["v"]
        a_ref[...] = -f["kk"]
        b_ref[...] = f["kk"] * f["a"]
        gate_ref[...] = f["gate"]

    vec = lambda w: pl.BlockSpec((1, w), lambda i: (0, 0))
    row = lambda w: pl.BlockSpec((tm, w), lambda i: (i, 0))
    return pl.pallas_call(
        body, name="rwkv_front", grid=(s // tm,),
        in_specs=[row(d), vec(d), pl.BlockSpec(w_a.shape, lambda i: (0, 0), pipeline_mode=pl.Buffered(1)),
                  vec(SEC), pl.BlockSpec((LANES, 2 * D_HALF), lambda i: (0, 0)),
                  vec(D_HALF), vec(D_HALF), vec(D_HALF), vec(D_HALF)],
        out_specs=[row(d), row(SEC)] + [row(D_HALF)] * 7,
        out_shape=[jax.ShapeDtypeStruct((s, d), BF16), jax.ShapeDtypeStruct((s, SEC), F32)]
                  + [jax.ShapeDtypeStruct((s, D_HALF), F32)] * 7,
        scratch_shapes=[pltpu.VMEM((1, SEC), F32)],
        compiler_params=_params("arbitrary"),
    )(x, g, w_a, mu, wl, w0, a0, kkw, kaw)


SCAN_TB = 128
N_PAIRS = 4


def _pair_sum(x, left):
    s_l = jnp.sum(jnp.where(left, x, 0.0), axis=1, keepdims=True)
    s_r = jnp.sum(jnp.where(left, 0.0, x), axis=1, keepdims=True)
    return jnp.where(left, s_l, s_r)


def _pair_dot(x, row_l, row_r, left):
    s_l = jnp.sum(x * row_l, axis=1, keepdims=True)
    s_r = jnp.sum(x * row_r, axis=1, keepdims=True)
    return jnp.where(left, s_l, s_r)


def _halves(rows8):
    lane = lax.broadcasted_iota(jnp.int32, rows8.shape, 1)
    keep_left = (lane & (LANES - 1)) < HEAD
    return jnp.where(keep_left, rows8, 0.0), jnp.where(keep_left, 0.0, rows8)


def _quad_consts():
    lane = lax.broadcasted_iota(jnp.int32, (HEAD, 2 * LANES), 1)
    rowi = lax.broadcasted_iota(jnp.int32, (HEAD, 2 * LANES), 0)
    diag2 = rowi == (lane & (HEAD - 1))
    r = lax.broadcasted_iota(jnp.int32, (2 * LANES, 2 * LANES), 0) >> 6
    c = lax.broadcasted_iota(jnp.int32, (2 * LANES, 2 * LANES), 1) >> 6
    return diag2, (r == c).astype(BF16)


def _rows_to_columns(x8, diag2, bd2):
    lhs = jnp.concatenate([jnp.where(diag2, x8[i:i + 1], 0.0).astype(BF16) for i in range(SUBLANES)], axis=0)
    return jnp.dot(lhs, bd2, preferred_element_type=F32)


def _diag_rows(qtile, diag2, bd2, sub_row2):
    res = jnp.dot(qtile, bd2, preferred_element_type=F32)
    out = jnp.zeros((SUBLANES, 2 * LANES), F32)
    for i in range(SUBLANES):
        out = jnp.where(sub_row2 == i, _colsum(jnp.where(diag2, res[i * HEAD:(i + 1) * HEAD], 0.0)), out)
    return out


def _store_tile(qbuf, slot, p, i, x):
    qbuf[slot, p // 2, i * HEAD:(i + 1) * HEAD, (p % 2) * LANES:(p % 2 + 1) * LANES] = x.astype(BF16)


def _left_half():
    return lax.broadcasted_iota(jnp.int32, (HEAD, LANES), 1) < HEAD


def _split_refs(refs, n_rows, n_out, exchange):
    n_in = len(exchange.operands) if exchange else 0
    n_ex_out = len(exchange.out_shapes) if exchange else 0
    refs = list(refs)
    rows, refs = refs[:n_rows], refs[n_rows:]
    ex_in, refs = refs[:n_in], refs[n_in:]
    outs, refs = refs[:n_out], refs[n_out:]
    ex_out, refs = refs[:n_ex_out], refs[n_ex_out:]
    scratch, sems = (refs[:-3], refs[-3:]) if exchange else (refs, None)
    moves = exchange.moves(ex_in, ex_out, sems) if exchange else None
    return rows, outs, scratch, moves


def _wkv_fwd(r, w, k, a, b, v, exchange=None):
    s = r.shape[0]
    tb = SCAN_TB
    nb = s // tb

    def body(*refs):
        (r_ref, w_ref, k_ref, a_ref, b_ref, v_ref), (y_ref, st_ref), (state, vbuf, qbuf), moves = _split_refs(
            refs, 6, 2, exchange)
        g = pl.program_id(0)
        if moves:
            moves.start(also=(g == 0))

        @pl.when(g == 0)
        def _():
            state[...] = jnp.zeros_like(state)
            qbuf[...] = jnp.zeros_like(qbuf)

        left = _left_half()
        diag2, bd2 = _quad_consts()
        sub_row2 = lax.broadcasted_iota(jnp.int32, (SUBLANES, 2 * LANES), 0)
        groups = tb // SUBLANES
        quads = [slice(g2 * 2 * LANES, (g2 + 1) * 2 * LANES) for g2 in range(2)]

        def rows_of(q):
            return pl.ds(pl.multiple_of(q * SUBLANES, SUBLANES), SUBLANES)

        def v_tiles(q, slot):
            v8 = v_ref[rows_of(q), :]
            for g2 in range(2):
                vbuf[slot, g2] = _rows_to_columns(v8[:, quads[g2]], diag2, bd2)

        def chain(q, slot):
            rows8 = rows_of(q)
            a8, w8, b8, k8, r8 = (x[rows8, :] for x in (a_ref, w_ref, b_ref, k_ref, r_ref))
            pairs = [slice(p * LANES, (p + 1) * LANES) for p in range(N_PAIRS)]
            a_next = pltpu.roll(a8, SUBLANES - 1, 0)
            (a8_l, a8_r), (wa8_l, wa8_r) = _halves(a8), _halves(w8 * a_next)
            ba8 =jnp.concatenate([_pair_sum(b8[:, pr] * a_next[:, pr], left[0:SUBLANES]) for pr in pairs], axis=1)
            ka8 = jnp.concatenate([_pair_sum(k8[:, pr] * a_next[:, pr], left[0:SUBLANES]) for pr in pairs], axis=1)
            sp = [state[p] for p in range(N_PAIRS)]
            for i in range(0, SUBLANES, 2):
                r0, r1 = slice(i, i + 1), slice(i + 1, i + 2)
                sums = [(_pair_dot(sp[p], a8_l[r0, pairs[p]], a8_r[r0, pairs[p]], left),
                         _pair_dot(sp[p], wa8_l[r0, pairs[p]], wa8_r[r0, pairs[p]], left)) for p in range(N_PAIRS)]
                sa0, sa1 = [s[0] for s in sums], [s[1] for s in sums]
                for p in range(N_PAIRS):
                    pr = pairs[p]
                    inner = slice((p % 2) * LANES, (p % 2 + 1) * LANES)
                    vt0 = vbuf[slot, p // 2, i * HEAD:(i + 1) * HEAD, inner]
                    vt1 = vbuf[slot, p // 2, (i + 1) * HEAD:(i + 2) * HEAD, inner]
                    sa_next = sa1[p] + sa0[p] * ba8[r0, pr] + vt0 * ka8[r0, pr]
                    s1 = sp[p] * w8[r0, pr] + sa0[p] * b8[r0, pr] + vt0 * k8[r0, pr]
                    st_ref[q * SUBLANES + i, p] = s1
                    _store_tile(qbuf, slot, p, i, s1 * r8[r0, pr])
                    s2 = s1 * w8[r1, pr] + sa_next * b8[r1, pr] + vt1 * k8[r1, pr]
                    st_ref[q * SUBLANES + i + 1, p] = s2
                    _store_tile(qbuf, slot, p, i + 1, s2 * r8[r1, pr])
                    sp[p] = s2
            for p in range(N_PAIRS):
                state[p] = sp[p]

        def y_rows(q, slot):
            for g2 in range(2):
                y_ref[rows_of(q), quads[g2]] = _diag_rows(qbuf[slot, g2], diag2, bd2, sub_row2)

        v_tiles(0, 0)

        def two_groups(j, carry):
            q0 = 2 * j
            v_tiles(q0 + 1, 1)
            chain(q0, 0)
            y_rows(jnp.maximum(q0 - 1, 0), 1)
            v_tiles(jnp.minimum(q0 + 2, groups - 1), 0)
            chain(q0 + 1, 1)
            y_rows(q0, 0)
            return carry

        lax.fori_loop(0, groups // 2, two_groups, 0)
        y_rows(groups - 1, 1)
        if moves:
            moves.wait(also=(g == nb - 1))

    rows = pl.BlockSpec((tb, D_HALF), lambda g: (g, 0))
    ex_in = exchange.operands if exchange else []
    ex_out = exchange.out_shapes if exchange else []
    res = pl.pallas_call(
        body, name="wkv_fwd", grid=(nb,),
        in_specs=[rows] * 6 + [ANY] * len(ex_in),
        out_specs=[rows, pl.BlockSpec((tb, N_PAIRS, HEAD, LANES), lambda g: (g, 0, 0, 0))] + [ANY] * len(ex_out),
        out_shape=[jax.ShapeDtypeStruct((s, D_HALF), F32),
                   jax.ShapeDtypeStruct((s, N_PAIRS, HEAD, LANES), F32)] + ex_out,
        scratch_shapes=[pltpu.VMEM((N_PAIRS, HEAD, LANES), F32),
                        pltpu.VMEM((2, 2, SUBLANES * HEAD, 2 * LANES), F32),
                        pltpu.VMEM((2, 2, SUBLANES * HEAD, 2 * LANES), BF16)]
                       + (exchange.scratch() if exchange else []),
        compiler_params=_params("arbitrary"),
    )(r, w, k, a, b, v, *ex_in)
    return res[0], res[1], list(res[2:])


def _wkv_bwd(r, w, k, a, b, v, dy, st, exchange=None):
    s = r.shape[0]
    tb = SCAN_TB
    nb = s // tb

    def body(*refs):
        ((r_ref, w_ref, k_ref, a_ref, b_ref, v_ref, dy_ref, st_ref, before_ref),
         (dr_ref, dw_ref, dk_ref, dv_ref, da_ref, db_ref), (dstate, vbuf, qbuf, sbuf),
         moves) = _split_refs(refs, 9, 6, exchange)
        g = pl.program_id(0)
        first_block = g == nb - 1
        if moves:
            moves.start(also=(g == 0))

        @pl.when(g == 0)
        def _():
            dstate[...] = jnp.zeros_like(dstate)
            qbuf[...] = jnp.zeros_like(qbuf)

        left = _left_half()
        diag2, bd2 = _quad_consts()
        sub_row = lax.broadcasted_iota(jnp.int32, (SUBLANES, LANES), 0)
        sub_row2 = lax.broadcasted_iota(jnp.int32, (SUBLANES, 2 * LANES), 0)
        groups = tb // SUBLANES
        quads = [slice(g2 * 2 * LANES, (g2 + 1) * 2 * LANES) for g2 in range(2)]
        row_refs = (dr_ref, dw_ref, dk_ref, da_ref, db_ref)

        def rows_of(q):
            return pl.ds(pl.multiple_of(q * SUBLANES, SUBLANES), SUBLANES)

        def state_before(q, i, p):
            if i > 0:
                return st_ref[q * SUBLANES + i - 1, p]
            return jnp.where(q == 0, jnp.where(first_block, 0.0, before_ref[0, p]),
                             st_ref[jnp.maximum(q * SUBLANES - 1, 0), p])

        def column_tiles(q, slot):
            rows8 = rows_of(q)
            for kind, ref in enumerate((v_ref, dy_ref)):
                x8 = ref[rows8, :]
                for g2 in range(2):
                    vbuf[slot, kind, g2] = _rows_to_columns(x8[:, quads[g2]], diag2, bd2)
            a8 = a_ref[rows8, :]
            for i in range(SUBLANES):
                for p in range(N_PAIRS):
                    _store_tile(sbuf, 0, p, i, state_before(q, i, p) * a8[i:i + 1, p * LANES:(p + 1) * LANES])
            for g2 in range(2):
                vbuf[slot, 2, g2] = jnp.dot(sbuf[0, g2], bd2, preferred_element_type=F32)

        def chain(q, slot):
            rows8 = rows_of(q)
            a8, w8, b8, k8, r8 = (x[rows8, :] for x in (a_ref, w_ref, b_ref, k_ref, r_ref))
            b8_l, b8_r = _halves(b8)
            dsp = [dstate[p] for p in range(N_PAIRS)]
            outs = [[jnp.zeros((SUBLANES, LANES), F32) for _ in row_refs] for _ in range(N_PAIRS)]
            after = [st_ref[q * SUBLANES + SUBLANES - 1, p] for p in range(N_PAIRS)]
            for i in reversed(range(SUBLANES)):
                row = slice(i, i + 1)
                pl_ = [slice(p * LANES, (p + 1) * LANES) for p in range(N_PAIRS)]
                tile = [(p // 2, slice(i * HEAD, (i + 1) * HEAD), slice((p % 2) * LANES, (p % 2 + 1) * LANES))
                        for p in range(N_PAIRS)]
                sp = [state_before(q, i, p) for p in range(N_PAIRS)]
                dyt = [vbuf[(slot, 1) + tile[p]] for p in range(N_PAIRS)]
                ds = [dsp[p] + dyt[p] * r8[row, pl_[p]] for p in range(N_PAIRS)]
                dsa = [_pair_dot(ds[p], b8_l[row, pl_[p]], b8_r[row, pl_[p]], left) for p in range(N_PAIRS)]
                sa = [vbuf[(slot, 2) + tile[p]] for p in range(N_PAIRS)]
                for p in range(N_PAIRS):
                    ar, wr, br, kr = (x[row, pl_[p]] for x in (a8, w8, b8, k8))
                    vt = vbuf[(slot, 0) + tile[p]]
                    dsp[p] = ds[p] * wr + dsa[p] * ar
                    new = (_colsum(after[p] * dyt[p]), _colsum(ds[p] * sp[p]), _colsum(ds[p] * vt),
                           _colsum(sp[p] * dsa[p]), _colsum(ds[p] * sa[p]))
                    outs[p] = [jnp.where(sub_row == i, n, o) for n, o in zip(new, outs[p])]
                    _store_tile(qbuf, slot, p, i, ds[p] * kr)
                after = sp
            for p in range(N_PAIRS):
                dstate[p] = dsp[p]
                for ref, o in zip(row_refs, outs[p]):
                    ref[rows8, p * LANES:(p + 1) * LANES] = o

        def dv_rows(q, slot):
            for g2 in range(2):
                dv_ref[rows_of(q), quads[g2]] = _diag_rows(qbuf[slot, g2], diag2, bd2, sub_row2)

        column_tiles(groups - 1, 0)

        def two_groups(j, carry):
            q0 = groups - 1 - 2 * j
            column_tiles(q0 - 1, 1)
            chain(q0, 0)
            dv_rows(jnp.minimum(q0 + 1, groups - 1), 1)
            column_tiles(jnp.maximum(q0 - 2, 0), 0)
            chain(q0 - 1, 1)
            dv_rows(q0, 0)
            return carry

        lax.fori_loop(0, groups // 2, two_groups, 0)
        dv_rows(0, 1)
        if moves:
            moves.wait(also=(g == nb - 1))

    rows = pl.BlockSpec((tb, D_HALF), lambda g: (nb - 1 - g, 0))
    ex_in = exchange.operands if exchange else []
    ex_out = exchange.out_shapes if exchange else []
    res = pl.pallas_call(
        body, name="wkv_bwd", grid=(nb,),
        in_specs=[rows] * 7 + [pl.BlockSpec((tb, N_PAIRS, HEAD, LANES), lambda g: (nb - 1 - g, 0, 0, 0)),
                               pl.BlockSpec((1, N_PAIRS, HEAD, LANES),
                                            lambda g: (jnp.maximum((nb - 1 - g) * tb - 1, 0), 0, 0, 0))]
                 + [ANY] * len(ex_in),
        out_specs=[rows] * 6 + [ANY] * len(ex_out),
        out_shape=[jax.ShapeDtypeStruct((s, D_HALF), F32)] * 6 + ex_out,
        scratch_shapes=[pltpu.VMEM((N_PAIRS, HEAD, LANES), F32),
                        pltpu.VMEM((2, 3, 2, SUBLANES * HEAD, 2 * LANES), F32),
                        pltpu.VMEM((2, 2, SUBLANES * HEAD, 2 * LANES), BF16),
                        pltpu.VMEM((1, 2, SUBLANES * HEAD, 2 * LANES), BF16)]
                       + (exchange.scratch() if exchange else []),
        compiler_params=_params("arbitrary"),
    )(r, w, k, a, b, v, dy, st, st, *ex_in)
    return list(res[:6]), list(res[6:])


def _rwkv_post_math(y, r, k, v, gate, lw, lb, rk, bd):
    mean = _head_sum(y, bd) * (1.0 / HEAD)
    yc = y - mean
    var = _head_sum(yc * yc, bd) * (1.0 / HEAD)
    rstd = lax.rsqrt(var + LNX_EPS)
    yn = yc * rstd
    rkk = _head_sum(r * k * rk, bd)
    sg = _sigmoid(gate)
    pre = yn * lw + lb + rkk * v
    return yn, rstd, rkk, sg, pre


def _rwkv_prep_bwd(u_a, h_t, grads, mu, wl, w0, a0, kkw, kaw, tm=256):
    s = u_a.shape[0]
    nb = s // tm
    d = h_t.shape[0]

    def body(ua_ref, prev_ref, ht_ref, drs_ref, dws_ref, dks_ref, dvs_ref, das_ref, dbs_ref, drb_ref, dkb_ref, dvb_ref,
             dgt_ref, mu_ref, wl_ref, w0_ref, a0_ref, kkw_ref, kaw_ref,
             du_ref, dwa_ref, dmu_ref, dwl_ref, dw0_ref, da0_ref, dkkw_ref, dkaw_ref, carry):
        i = pl.program_id(0)

        @pl.when(i == 0)
        def _():
            carry[...] = jnp.zeros_like(carry)
            for ref in (dwa_ref, dmu_ref, dwl_ref, dw0_ref, da0_ref, dkkw_ref, dkaw_ref):
                ref[...] = jnp.zeros_like(ref)

        bd = _head_ones()
        mu_v, wl_v, kkw_v, kaw_v = mu_ref[...], wl_ref[...], kkw_ref[...], kaw_ref[...]
        f = _rwkv_elementwise(ua_ref[...], prev_ref[7:8, :], i == nb - 1, mu_v, wl_v, w0_ref[...],
                              a0_ref[...], kkw_v, kaw_v, bd)
        a, kk, k0 = f["a"], f["kk"], f["k0"]
        dk = dks_ref[...] + dkb_ref[...]
        dbs = dbs_ref[...]
        dkk = dbs * a - das_ref[...]
        da = dbs * kk + dk * k0 * kaw_v
        dk0 = dk * (1.0 + (a - 1.0) * kaw_v)
        dkaw_ref[...] += _colsum(dk * k0 * (a - 1.0))
        inv = 1.0 / f["nrm"]
        proj = _head_sum(dkk * kk, bd)
        dkk0 = jnp.where(f["ss"] > 1e-24, (dkk - kk * proj) * inv, dkk * inv)
        dk0 = dk0 + dkk0 * kkw_v
        dkkw_ref[...] += _colsum(dkk0 * k0)
        dza = da * a * (1.0 - a)
        da0_ref[...] += _colsum(dza)
        dz = -dws_ref[...] * f["dec"] * f["e"] * (1.0 - f["sz"])
        dw0_ref[...] += _colsum(dz)
        dll = jnp.concatenate([dz, dza], axis=1).astype(BF16)
        dwl_ref[...] += _dot_tn(f["lin"].astype(BF16), dll)
        dlin = _dot_nt(dll, wl_v)
        lane = lax.broadcasted_iota(jnp.int32, (1, LANES), 1)
        th = f["th"]
        dlo = jnp.where(lane < LORA, dlin * (1.0 - th * th), dlin)
        dus = jnp.concatenate([drs_ref[...] + drb_ref[...], dk0, dvs_ref[...] + dvb_ref[...], dlo, dgt_ref[...]],
                              axis=1)
        dmu_ref[...] += _colsum(dus * f["delta"])
        g1 = dus * mu_v
        rows = lax.broadcasted_iota(jnp.int32, (tm, 1), 0)
        up = jnp.where(rows == tm - 1, carry[...], pltpu.roll(g1, tm - 1, 0))
        dua = dus - g1 + up
        du_ref[...] = dua
        dwa_ref[...] += jnp.dot(ht_ref[...], dua.astype(BF16), preferred_element_type=F32)
        carry[...] = g1[0:1, :]

    rev = lambda w: pl.BlockSpec((tm, w), lambda i: (nb - 1 - i, 0))
    vec = lambda w: pl.BlockSpec((1, w), lambda i: (0, 0))
    wl_spec = pl.BlockSpec((LANES, 2 * D_HALF), lambda i: (0, 0))
    return pl.pallas_call(
        body, name="rwkv_prep_bwd", grid=(nb,),
        in_specs=[rev(SEC), pl.BlockSpec((8, SEC), lambda i: (jnp.maximum((nb - 1 - i) * (tm // 8) - 1, 0), 0)),
                  pl.BlockSpec((d, tm), lambda i: (0, nb - 1 - i))]
                 + [rev(D_HALF)] * 10 + [vec(SEC), wl_spec] + [vec(D_HALF)] * 4,
        out_specs=[rev(SEC), pl.BlockSpec((d, SEC), lambda i: (0, 0)), vec(SEC), wl_spec] + [vec(D_HALF)] * 4,
        out_shape=[jax.ShapeDtypeStruct((s, SEC), F32), jax.ShapeDtypeStruct((d, SEC), F32),
                   jax.ShapeDtypeStruct((1, SEC), F32),
                   jax.ShapeDtypeStruct((LANES, 2 * D_HALF), F32)] + [jax.ShapeDtypeStruct((1, D_HALF), F32)] * 4,
        scratch_shapes=[pltpu.VMEM((1, SEC), F32)],
        compiler_params=_params("arbitrary"),
    )(u_a, u_a, h_t, *grads, mu, wl, w0, a0, kkw, kaw)


def _tri(tm, lower):
    r = lax.broadcasted_iota(jnp.int32, (tm, tm), 0)
    c = lax.broadcasted_iota(jnp.int32, (tm, tm), 1)
    return ((r >= c) if lower else (r <= c)).astype(BF16)


def _head_rms(x, g, bd):
    rinv = lax.rsqrt(_head_sum(x * x, bd) * (1.0 / HEAD) + RMS_EPS)
    xh = x * rinv
    return xh, rinv, xh * g


def _fox_front(h, w_b, fb, qg, kg, tm=256):
    s, d = h.shape

    def body(h_ref, wb_ref, fb_ref, qg_ref, kg_ref, ub_ref, q_ref, k_ref, v_ref, cc_ref, cr_ref, carry):
        i = pl.program_id(0)

        @pl.when(i == 0)
        def _():
            carry[...] = jnp.zeros_like(carry)

        ub_ref[...] = jnp.dot(h_ref[...], wb_ref[...], preferred_element_type=F32)
        bd = _head_ones()
        _, _, qn = _head_rms(ub_ref[:, 0:512], qg_ref[...], bd)
        _, _, kn = _head_rms(ub_ref[:, 512:1024], kg_ref[...], bd)
        q_ref[...] = (qn * ATT_SCALE).astype(BF16)
        k_ref[...] = kn.astype(BF16)
        v_ref[...] = ub_ref[:, 1024:1536].astype(BF16)
        lane = lax.broadcasted_iota(jnp.int32, (1, LANES), 1)
        logf = jnp.where(lane < N_HEADS, _log_sigmoid(ub_ref[:, 2048:2176] + fb_ref[...]), 0.0)
        cum = _exact_dot(logf, _tri(tm, True), ones_first=True) + carry[...]
        for h in range(N_HEADS):
            cc_ref[h] = jnp.broadcast_to(cum[:, h:h + 1], (tm, LANES))
        cr_ref[...] = jnp.transpose(cum)[0:N_HEADS, :]
        carry[...] = cum[tm - 1:tm, :]

    blk = pl.BlockSpec((tm, D_HALF), lambda i: (i, 0))
    return pl.pallas_call(
        body, name="fox_front", grid=(s // tm,),
        in_specs=[pl.BlockSpec((tm, d), lambda i: (i, 0)),
                  pl.BlockSpec(w_b.shape, lambda i: (0, 0), pipeline_mode=pl.Buffered(1)),
                  pl.BlockSpec((1, LANES), lambda i: (0, 0)),
                  pl.BlockSpec((1, D_HALF), lambda i: (0, 0)), pl.BlockSpec((1, D_HALF), lambda i: (0, 0))],
        out_specs=[pl.BlockSpec((tm, SEC), lambda i: (i, 0)), blk, blk, blk,
                   pl.BlockSpec((N_HEADS, tm, LANES), lambda i: (0, i, 0)), pl.BlockSpec((N_HEADS, tm), lambda i: (0, i))],
        out_shape=[jax.ShapeDtypeStruct((s, SEC), F32)] + [jax.ShapeDtypeStruct((s, D_HALF), BF16)] * 3
                  + [jax.ShapeDtypeStruct((N_HEADS, s, LANES), F32), jax.ShapeDtypeStruct((N_HEADS, s), F32)],
        scratch_shapes=[pltpu.VMEM((1, LANES), F32)],
        compiler_params=_params("arbitrary"),
    )(h, w_b, fb, qg, kg)


ATT_T = 256


def _tiles(nblk, by_query):
    if by_query:
        pairs = [(i, j) for i in range(nblk) for j in range(i + 1)]
    else:
        pairs = [(i, j) for j in range(nblk) for i in range(j, nblk)]
    return (jnp.asarray([p[0] for p in pairs], jnp.int32), jnp.asarray([p[1] for p in pairs], jnp.int32))


def _attn_fwd(q, k, v, cc, cr):
    s = q.shape[0]
    t = ATT_T
    nblk = s // t

    def body(qi_ref, kj_ref, q_ref, k_ref, v_ref, cc_ref, cr_ref, o_ref, lse_ref, m_sc, l_sc, acc_sc):
        i = qi_ref[pl.program_id(0)]
        j = kj_ref[pl.program_id(0)]

        @pl.when(j == 0)
        def _():
            m_sc[...] = jnp.full_like(m_sc, NEG)
            l_sc[...] = jnp.zeros_like(l_sc)
            acc_sc[...] = jnp.zeros_like(acc_sc)

        def tile(on_diagonal):
            causal = _causal_tile(t) if on_diagonal else None
            left = lax.broadcasted_iota(jnp.int32, (1, LANES), 1) < HEAD
            for p in range(N_PAIRS):
                lanes = slice(p * LANES, (p + 1) * LANES)
                q2, k2, v2 = q_ref[:, lanes], k_ref[:, lanes], v_ref[:, lanes]
                acc2 = acc_sc[:, lanes]
                for e in range(2):
                    h = 2 * p + e
                    msk = left if e == 0 else jnp.logical_not(left)
                    sc = _dot_nt(jnp.where(msk, q2, jnp.zeros_like(q2)), k2)
                    sc = sc + (_wide(cc_ref[h]) - cr_ref[h:h + 1, :])
                    if on_diagonal:
                        sc = jnp.where(causal, sc, NEG)
                    m_prev = m_sc[h]
                    m_new = jnp.maximum(m_prev, jnp.max(sc, axis=1, keepdims=True))
                    alpha = jnp.exp(m_prev - m_new)
                    pm = jnp.exp(sc - _wide(m_new))
                    l_sc[h] = alpha * l_sc[h] + jnp.sum(pm, axis=1, keepdims=True)
                    m_sc[h] = m_new
                    pv = jnp.dot(pm.astype(BF16), v2, preferred_element_type=F32)
                    acc2 = jnp.where(msk, alpha * acc2 + pv, acc2)
                acc_sc[:, lanes] = acc2

        pl.when(j < i)(functools.partial(tile, False))
        pl.when(j == i)(functools.partial(tile, True))

        @pl.when(j == i)
        def _():
            left = lax.broadcasted_iota(jnp.int32, (1, LANES), 1) < HEAD
            for p in range(N_PAIRS):
                lanes = slice(p * LANES, (p + 1) * LANES)
                inv = jnp.where(left, 1.0 / l_sc[2 * p], 1.0 / l_sc[2 * p + 1])
                o_ref[:, lanes] = acc_sc[:, lanes] * inv
            for h in range(N_HEADS):
                lse_ref[h] = m_sc[h] + jnp.log(l_sc[h])

    qi, kj = _tiles(nblk, by_query=True)
    qblk = pl.BlockSpec((t, D_HALF), lambda n, qi, kj: (qi[n], 0))
    kblk = pl.BlockSpec((t, D_HALF), lambda n, qi, kj: (kj[n], 0))
    qrep = pl.BlockSpec((N_HEADS, t, LANES), lambda n, qi, kj: (0, qi[n], 0))
    return pl.pallas_call(
        body, name="fox_attn_fwd",
        grid_spec=pltpu.PrefetchScalarGridSpec(
            num_scalar_prefetch=2, grid=(qi.shape[0],),
            in_specs=[qblk, kblk, kblk, qrep, pl.BlockSpec((N_HEADS, t), lambda n, qi, kj: (0, kj[n]))],
            out_specs=[qblk, qrep],
            scratch_shapes=[pltpu.VMEM((N_HEADS, t, LANES), F32), pltpu.VMEM((N_HEADS, t, LANES), F32),
                            pltpu.VMEM((t, D_HALF), F32)]),
        out_shape=[jax.ShapeDtypeStruct((s, D_HALF), F32), jax.ShapeDtypeStruct((N_HEADS, s, LANES), F32)],
        compiler_params=_params("arbitrary"),
    )(qi, kj, q, k, v, cc, cr)


def _causal_tile(t):
    return lax.broadcasted_iota(jnp.int32, (t, t), 0) >= lax.broadcasted_iota(jnp.int32, (t, t), 1)


def _wide(x):
    return jnp.concatenate([x, x], axis=1)


def _attn_probs(q2, k2, v2, do2, msk, causal, bias, lse_rows):
    zero = jnp.zeros_like(q2)
    qh = jnp.where(msk, q2, zero)
    doh = jnp.where(msk, do2, zero)
    sc = _dot_nt(qh, k2) + bias
    if causal is not None:
        sc = jnp.where(causal, sc, NEG)
    pm = jnp.exp(sc - _wide(lse_rows))
    dp = _dot_nt(doh, v2)
    return qh, doh, pm, dp


def _attn_bwd_rowdot(q, k, v, do, lse, cc, cr):
    s = q.shape[0]
    t = ATT_T
    nblk = s // t

    def body(qi_ref, kj_ref, q_ref, k_ref, v_ref, do_ref, lse_ref, cc_ref, cr_ref, dd_ref, acc):
        i = qi_ref[pl.program_id(0)]
        j = kj_ref[pl.program_id(0)]

        @pl.when(j == 0)
        def _():
            acc[...] = jnp.zeros_like(acc)

        def tile(on_diagonal):
            causal = _causal_tile(t) if on_diagonal else None
            left = lax.broadcasted_iota(jnp.int32, (1, LANES), 1) < HEAD
            for p in range(N_PAIRS):
                lanes = slice(p * LANES, (p + 1) * LANES)
                q2, k2, v2, do2 = q_ref[:, lanes], k_ref[:, lanes], v_ref[:, lanes], do_ref[:, lanes]
                for e in range(2):
                    h = 2 * p + e
                    msk = left if e == 0 else jnp.logical_not(left)
                    bias = _wide(cc_ref[h]) - cr_ref[h:h + 1, :]
                    _, _, pm, dp = _attn_probs(q2, k2, v2, do2, msk, causal, bias, lse_ref[h])
                    acc[h] += jnp.sum(pm * dp, axis=1, keepdims=True)

        pl.when(j < i)(functools.partial(tile, False))
        pl.when(j == i)(functools.partial(tile, True))

        @pl.when(j == i)
        def _():
            dd_ref[...] = acc[...]

    qi, kj = _tiles(nblk, by_query=True)
    qblk = pl.BlockSpec((t, D_HALF), lambda n, qi, kj: (qi[n], 0))
    qcol = pl.BlockSpec((N_HEADS, t, LANES), lambda n, qi, kj: (0, qi[n], 0))
    kblk = pl.BlockSpec((t, D_HALF), lambda n, qi, kj: (kj[n], 0))
    return pl.pallas_call(
        body, name="fox_attn_rowdot",
        grid_spec=pltpu.PrefetchScalarGridSpec(
            num_scalar_prefetch=2, grid=(qi.shape[0],),
            in_specs=[qblk, kblk, kblk, qblk, qcol, qcol, pl.BlockSpec((N_HEADS, t), lambda n, qi, kj: (0, kj[n]))],
            out_specs=qcol, scratch_shapes=[pltpu.VMEM((N_HEADS, t, LANES), F32)]),
        out_shape=jax.ShapeDtypeStruct((N_HEADS, s, LANES), F32),
        compiler_params=_params("arbitrary"),
    )(qi, kj, q, k, v, do, lse, cc, cr)


def _attn_bwd(q, k, v, do, lse, dd, cc, cr):
    s = q.shape[0]
    t = ATT_T
    nblk = s // t

    def body(qi_ref, kj_ref, q_ref, k_ref, v_ref, do_ref, lse_ref, dd_ref, cc_ref, cr_ref,
             dq_ref, dk_ref, dv_ref, dcr_ref, dk_sc, dv_sc, dcr_sc):
        i = qi_ref[pl.program_id(0)]
        j = kj_ref[pl.program_id(0)]

        @pl.when(pl.program_id(0) == 0)
        def _():
            dq_ref[...] = jnp.zeros_like(dq_ref)

        @pl.when(i == j)
        def _():
            dk_sc[...] = jnp.zeros_like(dk_sc)
            dv_sc[...] = jnp.zeros_like(dv_sc)
            dcr_sc[...] = jnp.zeros_like(dcr_sc)

        def tile(on_diagonal):
            causal = _causal_tile(t) if on_diagonal else None
            left = lax.broadcasted_iota(jnp.int32, (1, LANES), 1) < HEAD
            qrows = pl.ds(pl.multiple_of(i * t, t), t)
            for p in range(N_PAIRS):
                lanes = slice(p * LANES, (p + 1) * LANES)
                q2, k2, v2, do2 = q_ref[:, lanes], k_ref[:, lanes], v_ref[:, lanes], do_ref[:, lanes]
                zero = jnp.zeros_like(q2)
                dq2 = jnp.zeros((t, LANES), F32)
                dk2 = jnp.zeros((t, LANES), F32)
                dv2 = jnp.zeros((t, LANES), F32)
                for e in range(2):
                    h = 2 * p + e
                    msk = left if e == 0 else jnp.logical_not(left)
                    bias = _wide(cc_ref[h]) - cr_ref[h:h + 1, :]
                    qh, doh, pm, dp = _attn_probs(q2, k2, v2, do2, msk, causal, bias, lse_ref[h])
                    dsc = pm * (dp - _wide(dd_ref[h]))
                    dsb = dsc.astype(BF16)
                    dv2 += _dot_tn(pm.astype(BF16), doh)
                    dk2 += _dot_tn(dsb, qh)
                    dq2 += jnp.dot(dsb, jnp.where(msk, k2, zero), preferred_element_type=F32)
                    dcr_sc[h:h + 1, :] += -_colsum(dsc)
                dq_ref[qrows, lanes] += dq2 * ATT_SCALE
                dk_sc[:, lanes] += dk2
                dv_sc[:, lanes] += dv2

        pl.when(i > j)(functools.partial(tile, False))
        pl.when(i == j)(functools.partial(tile, True))

        @pl.when(i == nblk - 1)
        def _():
            dk_ref[...] = dk_sc[...]
            dv_ref[...] = dv_sc[...]
            dcr_ref[...] = dcr_sc[...]

    qi, kj = _tiles(nblk, by_query=False)
    qblk = pl.BlockSpec((t, D_HALF), lambda n, qi, kj: (qi[n], 0))
    qcol = pl.BlockSpec((N_HEADS, t, LANES), lambda n, qi, kj: (0, qi[n], 0))
    kblk = pl.BlockSpec((t, D_HALF), lambda n, qi, kj: (kj[n], 0))
    krow = pl.BlockSpec((N_HEADS, t), lambda n, qi, kj: (0, kj[n]))
    return pl.pallas_call(
        body, name="fox_attn_bwd",
        grid_spec=pltpu.PrefetchScalarGridSpec(
            num_scalar_prefetch=2, grid=(qi.shape[0],),
            in_specs=[qblk, kblk, kblk, qblk, qcol, qcol, qcol, krow],
            out_specs=[pl.BlockSpec((s, D_HALF), lambda n, qi, kj: (0, 0)), kblk, kblk, krow],
            scratch_shapes=[pltpu.VMEM((t, D_HALF), F32), pltpu.VMEM((t, D_HALF), F32), pltpu.VMEM((N_HEADS, t), F32)]),
        out_shape=[jax.ShapeDtypeStruct((s, D_HALF), F32)] * 3 + [jax.ShapeDtypeStruct((N_HEADS, s), F32)],
        compiler_params=_params("arbitrary"),
    )(qi, kj, q, k, v, do, lse, dd, cc, cr)


def _fox_prep_bwd(u_b, h_t, dq, dk, dv, dgate, dcum, fb, qg, kg, tm=256):
    s = u_b.shape[0]
    nb = s // tm
    d = h_t.shape[0]

    def body(ub_ref, ht_ref, dq_ref, dk_ref, dv_ref, dg_ref, dc_ref, fb_ref, qg_ref, kg_ref,
             du_ref, dwb_ref, dqg_ref, dkg_ref, dfb_ref, carry):
        i = pl.program_id(0)

        @pl.when(i == 0)
        def _():
            carry[...] = jnp.zeros_like(carry)
            dwb_ref[...] = jnp.zeros_like(dwb_ref)
            dqg_ref[...] = jnp.zeros_like(dqg_ref)
            dkg_ref[...] = jnp.zeros_like(dkg_ref)
            dfb_ref[...] = jnp.zeros_like(dfb_ref)

        bd = _head_ones()
        for lo, g_ref, d_ref, dgain_ref in ((0, qg_ref, dq_ref, dqg_ref), (512, kg_ref, dk_ref, dkg_ref)):
            gain = g_ref[...]
            xh, rinv, _ = _head_rms(ub_ref[:, lo:lo + 512], gain, bd)
            dn = d_ref[...]
            dgain_ref[...] += _colsum(dn * xh)
            dxh = dn * gain
            du_ref[:, lo:lo + 512] = rinv * (dxh - xh * (_head_sum(dxh * xh, bd) * (1.0 / HEAD)))
        du_ref[:, 1024:1536] = dv_ref[...]
        du_ref[:, 1536:2048] = dg_ref[...]
        lane = lax.broadcasted_iota(jnp.int32, (1, LANES), 1)
        dc = dc_ref[...]
        dlogf = _exact_dot(dc, _tri(tm, False), ones_first=True) + carry[...]
        carry[...] += _colsum(dc)
        fl = ub_ref[:, 2048:2176] + fb_ref[...]
        dfl = jnp.where(lane < N_HEADS, dlogf * (1.0 - _sigmoid(fl)), 0.0)
        du_ref[:, 2048:2176] = dfl
        dfb_ref[...] += _colsum(dfl)
        dwb_ref[...] += jnp.dot(ht_ref[...], du_ref[...].astype(BF16), preferred_element_type=F32)

    rev = lambda w: pl.BlockSpec((tm, w), lambda i: (nb - 1 - i, 0))
    vec = lambda w: pl.BlockSpec((1, w), lambda i: (0, 0))
    return pl.pallas_call(
        body, name="fox_prep_bwd", grid=(nb,),
        in_specs=[rev(SEC), pl.BlockSpec((d, tm), lambda i: (0, nb - 1 - i))] + [rev(D_HALF)] * 4
                 + [rev(LANES), vec(LANES), vec(D_HALF), vec(D_HALF)],
        out_specs=[rev(SEC), pl.BlockSpec((d, SEC), lambda i: (0, 0)), vec(D_HALF), vec(D_HALF), vec(LANES)],
        out_shape=[jax.ShapeDtypeStruct((s, SEC), F32), jax.ShapeDtypeStruct((d, SEC), F32),
                   jax.ShapeDtypeStruct((1, D_HALF), F32), jax.ShapeDtypeStruct((1, D_HALF), F32),
                   jax.ShapeDtypeStruct((1, LANES), F32)],
        scratch_shapes=[pltpu.VMEM((1, LANES), F32)],
        compiler_params=_params("arbitrary"),
    )(u_b, h_t, dq, dk, dv, dgate, dcum, fb, qg, kg)


def _merge(y, r, k, v, gate_a, o, u_b, h, x, tgt, w_g, wa, wb, wo, fg, lw, lb, rk, tm=256):
    s, d = x.shape

    def body(y_ref, r_ref, k_ref, v_ref, ga_ref, o_ref, gb_ref, h_ref, x_ref, t_ref, wg_ref, wa_ref, wb_ref, wo_ref,
             fg_ref, lw_ref, lb_ref, rk_ref,
             dx2_ref, dy_ref, drb_ref, dkb_ref, dvb_ref, dga_ref, do_ref, dgb_ref, dug_ref,
             dwa_ref, dwb_ref, dwo_ref, dfg_ref, loss_ref, dlw_ref, dlb_ref, drk_ref):
        i = pl.program_id(0)

        @pl.when(i == 0)
        def _():
            for ref in (dwa_ref, dwb_ref, dwo_ref, dfg_ref, loss_ref, dlw_ref, dlb_ref, drk_ref):
                ref[...] = jnp.zeros_like(ref)

        bd = _head_ones()
        wa_v, wb_v, wo_v, fg_v = wa_ref[...], wb_ref[...], wo_ref[...], fg_ref[...]
        rv, kv, vv, ga, lw_v, rk_v = r_ref[...], k_ref[...], v_ref[...], ga_ref[...], lw_ref[...], rk_ref[...]
        yn, rstd, rkk, sga, pre = _rwkv_post_math(y_ref[...], rv, kv, vv, ga, lw_v, lb_ref[...], rk_v, bd)
        silu_a = ga * sga
        gb, ov = gb_ref[...], o_ref[...]
        sgb = _sigmoid(gb)
        silu_b = gb * sgb
        ma = (pre * silu_a).astype(BF16)
        mb = (ov * silu_b).astype(BF16)
        ya = jnp.dot(ma, wa_v, preferred_element_type=F32)
        yb = jnp.dot(mb, wb_v, preferred_element_type=F32)
        ug = jnp.dot(h_ref[...], wg_ref[...], preferred_element_type=F32)
        sa = _sigmoid(ug[:, 0:d])
        sb = _sigmoid(ug[:, d:2 * d])
        merged = (sa * ya + sb * yb).astype(BF16)
        x2 = x_ref[...] + jnp.dot(merged, wo_v, preferred_element_type=F32)
        r2 = lax.rsqrt(jnp.mean(x2 * x2, axis=-1, keepdims=True) + RMS_EPS)
        x2h = x2 * r2
        err = x2h * fg_v - t_ref[...]
        loss_ref[...] += _colsum(err * err)
        dyo = err * (1.0 / d)
        dfg_ref[...] += _colsum(dyo * x2h)
        dx2h = dyo * fg_v
        dx2 = r2 * (dx2h - x2h * jnp.mean(dx2h * x2h, axis=-1, keepdims=True))
        dx2_ref[...] = dx2
        dx2b = dx2.astype(BF16)
        dmerged = _dot_nt(dx2b, wo_v)
        dwo_ref[...] += _dot_tn(merged, dx2b)
        dya = dmerged * sa
        dyb = dmerged * sb
        dug_ref[:, 0:d] = dya * ya * (1.0 - sa)
        dug_ref[:, d:2 * d] = dyb * yb * (1.0 - sb)
        dyab = dya.astype(BF16)
        dybb = dyb.astype(BF16)
        dwa_ref[...] += _dot_tn(ma, dyab)
        dwb_ref[...] += _dot_tn(mb, dybb)
        dmb = _dot_nt(dybb, wb_v)
        do_ref[...] = (dmb * silu_b).astype(BF16)
        dgb_ref[...] = dmb * ov * (sgb * (1.0 + gb * (1.0 - sgb)))
        dma = _dot_nt(dyab, wa_v)
        dga_ref[...] = dma * pre * (sga * (1.0 + ga * (1.0 - sga)))
        dpre = dma * silu_a
        dlw_ref[...] += _colsum(dpre * yn)
        dlb_ref[...] += _colsum(dpre)
        dyn = dpre * lw_v
        m1 = _head_sum(dyn, bd) * (1.0 / HEAD)
        m2 = _head_sum(dyn * yn, bd) * (1.0 / HEAD)
        dy_ref[...] = rstd * (dyn - m1 - yn * m2)
        dvb_ref[...] = dpre * rkk
        drkk = _head_sum(dpre * vv, bd)
        drb_ref[...] = drkk * kv * rk_v
        dkb_ref[...] = drkk * rv * rk_v
        drk_ref[...] += _colsum(drkk * rv * kv)

    row = lambda w: pl.BlockSpec((tm, w), lambda i: (i, 0))
    full = lambda a: pl.BlockSpec(a.shape, lambda i: (0, 0))
    once = lambda a: pl.BlockSpec(a.shape, lambda i: (0, 0), pipeline_mode=pl.Buffered(1))
    half = jax.ShapeDtypeStruct((s, D_HALF), F32)
    fshape = lambda a: jax.ShapeDtypeStruct(a.shape, F32)
    return pl.pallas_call(
        body, name="merge_fwd_bwd", grid=(s // tm,),
        in_specs=[row(D_HALF)] * 6 + [pl.BlockSpec((tm, D_HALF), lambda i: (i, 3)), row(d), row(d), row(d),
                                      once(w_g), once(wa), once(wb), once(wo), full(fg), full(lw), full(lb), full(rk)],
        out_specs=[row(d)] + [row(D_HALF)] * 7 + [row(GATE_COLS), full(wa), full(wb), full(wo), full(fg), full(fg),
                                                   full(lw), full(lb), full(rk)],
        out_shape=[jax.ShapeDtypeStruct((s, d), F32)] + [half] * 5 + [jax.ShapeDtypeStruct((s, D_HALF), BF16), half,
                                                                    jax.ShapeDtypeStruct((s, GATE_COLS), F32),
                                                                    fshape(wa), fshape(wb), fshape(wo), fshape(fg),
                                                                    fshape(fg), fshape(lw), fshape(lb), fshape(rk)],
        compiler_params=_params("arbitrary"),
    )(y, r, k, v, gate_a, o, u_b, h, x, tgt, w_g, wa, wb, wo, fg, lw, lb, rk)


def _lora_weight(w_up, a_up):
    z = jnp.zeros((LORA, D_HALF), w_up.dtype)
    return jnp.concatenate([jnp.concatenate([w_up, z], axis=1), jnp.concatenate([z, a_up], axis=1)], axis=0)


def _device_grads(x, tgt, p, w_a, w_up, a_up, late_weights, fwd_exchange=None, bwd_exchange=None, tail_exchange=None):
    wl = _lora_weight(w_up, a_up)
    rk = p["r_k"].reshape(1, D_HALF)
    fb = jnp.pad(p["f_bias"], ((0, 0), (0, LANES - N_HEADS)))
    qg = jnp.tile(p["q_norm_g"], (1, N_HEADS))
    kg = jnp.tile(p["k_norm_g"], (1, N_HEADS))
    fg = p["final_norm_g"].reshape(1, D_MODEL)
    mixer = (p["shift_mu"], wl, p["w0"], p["a0"], p["k_k"], p["k_a"])

    h, u_a, r, dec, k, v, av, bv, gate_a = _rwkv_front(x, p["norm_g"], w_a, *mixer)
    y, st, arrived = _wkv_fwd(r, dec, k, av, bv, v, fwd_exchange)

    w_b, w_g, w_out_a, w_out_b, w_out = late_weights(arrived)
    u_b, q, kn, vb, cc, cr = _fox_front(h, w_b, fb, qg, kg)
    o, lse = _attn_fwd(q, kn, vb, cc, cr)

    (dx2, dy, dr_b, dk_b, dv_b, dgate_a, do, dgate_b, du_g, dwa, dwb, dwo, dfg, loss_vec, dlw, dlb, drk) = _merge(
        y, r, k, v, gate_a, o, u_b, h, x, tgt, w_g, w_out_a, w_out_b, w_out, fg, p["lnx_w"], p["lnx_b"], rk)

    dd = _attn_bwd_rowdot(q, kn, vb, do, lse, cc, cr)
    dq, dk_att, dv_att, dcr = _attn_bwd(q, kn, vb, do, lse, dd, cc, cr)
    dcum = jnp.pad(dcr.T, ((0, 0), (0, LANES - N_HEADS)))
    h_t = h.T
    du_b, dw_b, dqg, dkg, dfb = _fox_prep_bwd(u_b, h_t, dq, dk_att, dv_att, dgate_b, dcum, fb, qg, kg)
    dw_g = _matmul_tn_acc(h_t, du_g, "dw_gate")

    scan_grads, sent = _wkv_bwd(r, dec, k, av, bv, v, dy, st,
                                bwd_exchange(dw_b, dw_g, dwa, dwb, dwo) if bwd_exchange else None)
    du_a, dw_a, dmu, dwl, dw0, da0, dkkw, dkaw = _rwkv_prep_bwd(
        u_a, h_t, (*scan_grads, dr_b, dk_b, dv_b, dgate_a), *mixer)
    dw_up, da_up = dwl[:LORA, :D_HALF], dwl[LORA:, D_HALF:]
    sent_last = _run_on_sequencer(tail_exchange(dw_a, dw_up, da_up), "scatter_tail", 1) if tail_exchange else []
    grad_x, dnorm_g, _ = _inproj_bwd(du_a, du_b, du_g, w_a, w_b, w_g, x, dx2, p["norm_g"])

    grads = dict(
        norm_g=dnorm_g, w_in=(dw_a, dw_b, dw_g), shift_mu=dmu,
        w_lora_up=dw_up, w0=dw0, a_lora_up=da_up, a0=da0, k_k=dkkw, k_a=dkaw,
        r_k=drk.reshape(1, N_HEADS, HEAD), lnx_w=dlw, lnx_b=dlb, f_bias=dfb[:, :N_HEADS],
        q_norm_g=dqg.reshape(N_HEADS, HEAD).sum(axis=0, keepdims=True),
        k_norm_g=dkg.reshape(N_HEADS, HEAD).sum(axis=0, keepdims=True),
        w_out_a=dwa, w_out_b=dwb, w_out=dwo, final_norm_g=dfg.reshape(D_MODEL))
    return loss_vec, grad_x, grads, sent, sent_last


CHIP_FLIPS = ((1, 0), (0, 1), (1, 1))
ANY = pl.BlockSpec(memory_space=pl.ANY)


def _position():
    return lax.axis_index("x"), lax.axis_index("y"), lax.axis_index("c")


def _flip(v, f):
    return 1 - v if f else v


def _both(a, b):
    if a is None:
        return b
    return a if b is None else jnp.logical_and(a, b)


def _when(cond, fn):
    if cond is None:
        fn()
    else:
        pl.when(cond)(fn)


class _Moves:
    def __init__(self, send_sems, recv_sems, local_sems):
        self.send_sems, self.recv_sems, self.local_sems = send_sems, recv_sems, local_sems
        self.remote, self.local = [], []

    def send(self, src, dst, peer, landing, send_if=None, recv_if=None, first=False):
        k = len(self.remote)
        sems = dict(send_sem=self.send_sems.at[k], recv_sem=self.recv_sems.at[k], device_id=peer, device_id_type=MESH)
        out = pltpu.make_async_remote_copy(src_ref=src, dst_ref=dst, **sems)
        arrival = pltpu.make_async_remote_copy(src_ref=src, dst_ref=landing, **sems)
        self.remote.append((out, arrival, send_if, recv_if, first))

    def copy(self, src, dst, cond=None):
        cp = pltpu.make_async_copy(src, dst, self.local_sems.at[len(self.local)])
        self.local.append((cp, cond))

    def start(self, also=None):
        for cp, cond in self.local:
            _when(_both(also, cond), cp.start)
        for out, _, send_if, _, _ in self.remote:
            _when(_both(also, send_if), out.start)

    def wait_arrivals(self, also=None, first=None):
        for _, arrival, _, recv_if, is_first in self.remote:
            if first is None or first == is_first:
                _when(_both(also, recv_if), arrival.wait_recv)

    def wait_sent(self, also=None):
        for out, _, send_if, _, _ in self.remote:
            _when(_both(also, send_if), out.wait_send)
        for cp, cond in self.local:
            _when(_both(also, cond), cp.wait)

    def wait(self, also=None):
        self.wait_arrivals(also)
        self.wait_sent(also)


class _Exchange:
    def __init__(self, operands, out_shapes, n_remote, n_local, build, relays=None, in_place=(), n_staging=0):
        self.operands, self.out_shapes = list(operands), list(out_shapes)
        self.n_remote, self.n_local, self.build = n_remote, n_local, build
        self.relays, self.in_place = relays, in_place
        self.n_staging = n_staging

    def scratch(self):
        return [pltpu.SemaphoreType.DMA((self.n_remote,)), pltpu.SemaphoreType.DMA((self.n_remote,)),
                pltpu.SemaphoreType.DMA((max(self.n_local, 1),))]

    def moves(self, in_refs, out_refs, sems):
        mv = _Moves(*sems)
        self.build(mv, in_refs, out_refs)
        return mv


def _run_on_sequencer(exchange, name, collective_id):
    ins = [jax.new_ref(a, memory_space=pltpu.MemorySpace.HBM) for a in exchange.operands]
    outs = [ins[i] if i in exchange.in_place else jax.empty_ref(s, memory_space=pltpu.MemorySpace.HBM)
            for i, s in enumerate(exchange.out_shapes)]
    forward, to_sibling = exchange.relays or (None, None)
    relay_scratch = [pltpu.SemaphoreType.DMA((stage[0],)) for stage in (forward, to_sibling) if stage for _ in range(2)]

    def launch(*sems):
        x, y, c = _position()
        peers = [(_flip(x, fx), _flip(y, fy), c) for fx, fy in CHIP_FLIPS] + ([(x, y, 1 - c)] if to_sibling else [])
        barrier = pltpu.get_barrier_semaphore()
        for peer in peers:
            pl.semaphore_signal(barrier, inc=1, device_id=peer, device_id_type=MESH)
        pl.semaphore_wait(barrier, len(peers))
        moves = exchange.moves(ins, outs, sems[:3])
        moves.start()
        later = []
        if forward:
            onward = _Moves(sems[3], sems[4], None)
            forward[1](onward, ins, outs)
            moves.wait_arrivals(first=True)
            onward.start()
            moves.wait_arrivals(first=False)
            onward.wait_arrivals()
            later.append(onward)
        else:
            moves.wait_arrivals()
        if to_sibling:
            passed = _Moves(*sems[-2:], None)
            to_sibling[1](passed, ins, outs)
            passed.start()
            passed.wait_arrivals()
            later.append(passed)
        for mv in later + [moves]:
            mv.wait_sent()

    pl.kernel(launch, mesh=plsc.ScalarSubcoreMesh(axis_name="sequencer", num_cores=1), name=name,
              scratch_types=tuple(exchange.scratch() + relay_scratch),
              compiler_params=pltpu.CompilerParams(collective_id=collective_id))()
    return [o[...] for o in outs[:len(outs) - exchange.n_staging]]


def _row_major_copy(a, name):
    r, c = a.shape
    tr = _row_tile(r)

    def body(a_ref, o_ref):
        o_ref[...] = a_ref[...]

    blk = pl.BlockSpec((tr, c), lambda i: (i, 0))
    return pl.pallas_call(body, name=name, grid=(r // tr,), in_specs=[blk], out_specs=blk,
                          out_shape=jax.ShapeDtypeStruct(a.shape, a.dtype), compiler_params=_params("parallel"))(a)


def _is_chip(x, y, chip):
    return jnp.logical_and(x == chip // 2, y == chip % 2)


def _gather_exchange(from_chip, from_all, split=()):
    n1, n2 = len(from_chip), len(from_all)
    near = CHIP_FLIPS[:2]

    def quarters(t, c, first, count=1):
        n = from_chip[t][1].shape[0] // 4
        return pl.ds((2 * c + first) * n, count * n)

    def build(mv, ins, outs):
        x, y, c = _position()
        me = 2 * x + y
        for t, (chip, _) in enumerate(from_chip):
            if t not in split:
                mv.copy(ins[t], outs[t], cond=_is_chip(x, y, chip))
        for t in range(n2):
            mv.copy(ins[n1 + t], outs[n1 + t].at[me])
        for t in split:
            for first in (True, False):
                for f, (fx, fy) in enumerate(near):
                    px, py = _flip(x, fx), _flip(y, fy)
                    part = quarters(t, c, f if first else 1 - f)
                    mv.send(ins[t].at[part], outs[t].at[part], (px, py, c), landing=outs[t].at[part], first=first,
                            send_if=_is_chip(x, y, from_chip[t][0]), recv_if=_is_chip(px, py, from_chip[t][0]))
        for fx, fy in CHIP_FLIPS:
            px, py = _flip(x, fx), _flip(y, fy)
            peer = (px, py, c)
            for t, (chip, _) in enumerate(from_chip):
                if t not in split:
                    mv.send(ins[t], outs[t], peer, landing=outs[t],
                            send_if=_is_chip(x, y, chip), recv_if=_is_chip(px, py, chip))
            for t in range(n2):
                mv.send(ins[n1 + t], outs[n1 + t].at[me], peer, landing=outs[n1 + t].at[2 * px + py])

    def forward(mv, ins, outs):
        x, y, c = _position()
        for t in split:
            chip = from_chip[t][0]
            for f, (fx, fy) in enumerate(near):
                gx, gy = near[1 - f]
                part = quarters(t, c, f)
                mv.send(outs[t].at[part], outs[t].at[part], (_flip(x, gx), _flip(y, gy), c), landing=outs[t].at[part],
                        send_if=_is_chip(_flip(x, fx), _flip(y, fy), chip), recv_if=_is_chip(1 - x, 1 - y, chip))

    def to_sibling(mv, ins, outs):
        x, y, c = _position()
        for t in split:
            came = jnp.logical_not(_is_chip(x, y, from_chip[t][0]))
            mv.send(outs[t].at[quarters(t, c, 0, 2)], outs[t].at[quarters(t, c, 0, 2)], (x, y, 1 - c),
                    landing=outs[t].at[quarters(t, 1 - c, 0, 2)], send_if=came, recv_if=came)

    arrays = [a for _, a in from_chip] + list(from_all)
    shapes = [jax.ShapeDtypeStruct(a.shape, a.dtype) for _, a in from_chip]
    shapes += [jax.ShapeDtypeStruct((N_CHIPS,) + a.shape, a.dtype) for a in from_all]
    n_remote = len(CHIP_FLIPS) * (n1 - len(split) + n2) + 2 * len(near) * len(split)
    relays = ((len(near) * len(split), forward), (len(split), to_sibling)) if split else None
    return _Exchange(arrays, shapes, n_remote, n1 + n2, build, relays, in_place=split)


def _scatter_exchange(to_chip, to_all, via_neighbours=False):
    n1, n2 = len(to_chip), len(to_all)
    near = CHIP_FLIPS[:2]
    direct = near if via_neighbours else CHIP_FLIPS

    def half(t, g):
        n = to_chip[t][1].shape[0] // 2
        return pl.ds(g * n, n)

    def build(mv, ins, outs):
        x, y, c = _position()
        if via_neighbours:
            for t, (chip, _) in enumerate(to_chip):
                for g, (gx, gy) in enumerate(near):
                    ox, oy = near[1 - g]
                    mv.send(ins[t].at[half(t, g)], outs[n1 + n2 + t], (_flip(x, gx), _flip(y, gy), c),
                            landing=outs[n1 + n2 + t], first=True, send_if=_is_chip(1 - x, 1 - y, chip),
                            recv_if=_is_chip(_flip(x, ox), _flip(y, oy), chip))
        for f, (fx, fy) in enumerate(CHIP_FLIPS):
            px, py = _flip(x, fx), _flip(y, fy)
            peer = (px, py, c)
            if (fx, fy) in direct:
                for t, (chip, _) in enumerate(to_chip):
                    mv.send(ins[t], outs[t].at[f], peer, landing=outs[t].at[f],
                            send_if=_is_chip(px, py, chip), recv_if=_is_chip(x, y, chip))
            for t in range(n2):
                mv.send(ins[n1 + t].at[2 * px + py], outs[n1 + t].at[f], peer, landing=outs[n1 + t].at[f])

    def forward(mv, ins, outs):
        x, y, c = _position()
        for t, (chip, _) in enumerate(to_chip):
            for g in range(len(near)):
                ox, oy = near[1 - g]
                far_slot = outs[t].at[len(near)].at[half(t, g)]
                mv.send(outs[n1 + n2 + t], far_slot, (_flip(x, ox), _flip(y, oy), c), landing=far_slot,
                        send_if=_is_chip(_flip(x, ox), _flip(y, oy), chip), recv_if=_is_chip(x, y, chip))

    arrays = [a for _, a in to_chip] + list(to_all)
    shapes = [jax.ShapeDtypeStruct((len(CHIP_FLIPS),) + a.shape, a.dtype) for _, a in to_chip]
    shapes += [jax.ShapeDtypeStruct((len(CHIP_FLIPS),) + a.shape[1:], a.dtype) for a in to_all]
    if not via_neighbours:
        return _Exchange(arrays, shapes, len(CHIP_FLIPS) * (n1 + n2), 0, build)
    shapes += [jax.ShapeDtypeStruct((a.shape[0] // 2, a.shape[1]), a.dtype) for _, a in to_chip]
    return _Exchange(arrays, shapes, 2 * len(near) * n1 + len(CHIP_FLIPS) * n2, 0, build,
                     relays=((len(near) * n1, forward), None), n_staging=n1)


def _swap_sibling(tensors, name):
    n = len(tensors)

    def body(*refs):
        ins, outs = refs[:n], refs[n:2 * n]
        send_sems, recv_sems = refs[2 * n:]
        x, y, c = _position()
        copies = [pltpu.make_async_remote_copy(
            src_ref=ins[t], dst_ref=outs[t], send_sem=send_sems.at[t], recv_sem=recv_sems.at[t],
            device_id=(x, y, 1 - c), device_id_type=MESH) for t in range(n)]
        for cp in copies:
            cp.start()
        for cp in copies:
            cp.wait_recv()
        for cp in copies:
            cp.wait_send()

    return pl.pallas_call(
        body, name=name, in_specs=[ANY] * n, out_specs=[ANY] * n,
        out_shape=[jax.ShapeDtypeStruct(a.shape, a.dtype) for a in tensors],
        scratch_shapes=[pltpu.SemaphoreType.DMA((n,)), pltpu.SemaphoreType.DMA((n,))],
        compiler_params=pltpu.CompilerParams(has_side_effects=True),
    )(*tensors)


def _pair_halves(g):
    r, cols = g.shape
    half = r // 2

    def body(g_ref, o_ref, mine, theirs, send_sem, recv_sem, local_sem):
        x, y, c = _position()
        away = pltpu.make_async_remote_copy(
            src_ref=g_ref.at[pl.ds((1 - c) * half, half)], dst_ref=theirs, send_sem=send_sem, recv_sem=recv_sem,
            device_id=(x, y, 1 - c), device_id_type=MESH)
        kept = pltpu.make_async_copy(g_ref.at[pl.ds(c * half, half)], mine, local_sem)
        away.start()
        kept.start()
        kept.wait()
        away.wait_recv()
        o_ref[...] = (mine[...] + theirs[...]).astype(BF16)
        away.wait_send()

    return pl.pallas_call(
        body, name="pair_halves", in_specs=[ANY], out_specs=pl.BlockSpec(memory_space=pltpu.VMEM),
        out_shape=jax.ShapeDtypeStruct((half, cols), BF16),
        scratch_shapes=[pltpu.VMEM((half, cols), F32), pltpu.VMEM((half, cols), F32),
                        pltpu.SemaphoreType.DMA(()), pltpu.SemaphoreType.DMA(()), pltpu.SemaphoreType.DMA(())],
        compiler_params=pltpu.CompilerParams(has_side_effects=True, vmem_limit_bytes=VMEM_LIMIT),
    )(g)


def _allreduce_small(slab):
    stages = 3

    def body(x_ref, o_ref, buf, send_sems, recv_sems):
        x, y, c = _position()
        peers = ((1 - x, y, c), (x, 1 - y, c), (x, y, 1 - c))
        o_ref[...] = x_ref[...]
        for k, peer in enumerate(peers):
            cp = pltpu.make_async_remote_copy(src_ref=o_ref, dst_ref=buf.at[k], send_sem=send_sems.at[k],
                                              recv_sem=recv_sems.at[k], device_id=peer, device_id_type=MESH)
            cp.start()
            cp.wait()
            o_ref[...] = o_ref[...] + buf[k]

    return pl.pallas_call(
        body, name="allreduce_small",
        in_specs=[pl.BlockSpec(memory_space=pltpu.VMEM)], out_specs=pl.BlockSpec(memory_space=pltpu.VMEM),
        out_shape=jax.ShapeDtypeStruct(slab.shape, slab.dtype),
        scratch_shapes=[pltpu.VMEM((stages,) + slab.shape, slab.dtype),
                        pltpu.SemaphoreType.DMA((stages,)), pltpu.SemaphoreType.DMA((stages,))],
        compiler_params=pltpu.CompilerParams(has_side_effects=True),
    )(slab)


def _row_tile(r):
    return min(r, 256)


def _sum4(stack, recv, me):
    _, r, c = stack.shape
    tr = _row_tile(r)

    def body(me_ref, own_ref, recv_ref, o_ref):
        o_ref[...] = (((own_ref[...] + recv_ref[0].astype(F32)) + recv_ref[1].astype(F32))
                      + recv_ref[2].astype(F32))

    return pl.pallas_call(
        body, name="sum_partials",
        grid_spec=pltpu.PrefetchScalarGridSpec(
            num_scalar_prefetch=1, grid=(r // tr,),
            in_specs=[pl.BlockSpec((None, tr, c), lambda i, me_ref: (me_ref[0], i, 0)),
                      pl.BlockSpec((len(CHIP_FLIPS), tr, c), lambda i, me_ref: (0, i, 0))],
            out_specs=pl.BlockSpec((tr, c), lambda i, me_ref: (i, 0))),
        out_shape=jax.ShapeDtypeStruct((r, c), F32), compiler_params=_params("parallel"),
    )(me, stack, recv)


def _sum_block(own, recv):
    r, c = own.shape
    tr = _row_tile(r)

    def body(own_ref, recv_ref, o_ref):
        o_ref[...] = (((own_ref[...] + recv_ref[0].astype(F32)) + recv_ref[1].astype(F32))
                      + recv_ref[2].astype(F32))

    return pl.pallas_call(
        body, name="sum_block", grid=(r // tr,),
        in_specs=[pl.BlockSpec((tr, c), lambda i: (i, 0)), pl.BlockSpec((len(CHIP_FLIPS), tr, c), lambda i: (0, i, 0))],
        out_specs=pl.BlockSpec((tr, c), lambda i: (i, 0)),
        out_shape=jax.ShapeDtypeStruct((r, c), F32), compiler_params=_params("parallel"),
    )(own, recv)


def _sum_half(own, recv, core):
    r, c = own.shape
    tr = _row_tile(r // 2)
    per_half = r // 2 // tr

    def body(core_ref, own_ref, recv_ref, o_ref):
        mine = pl.program_id(0) // per_half == core_ref[0]

        @pl.when(mine)
        def _():
            o_ref[...] = (((own_ref[...] + recv_ref[0].astype(F32)) + recv_ref[1].astype(F32))
                          + recv_ref[2].astype(F32))

        @pl.when(jnp.logical_not(mine))
        def _():
            o_ref[...] = own_ref[...]

    return pl.pallas_call(
        body, name="sum_half",
        grid_spec=pltpu.PrefetchScalarGridSpec(
            num_scalar_prefetch=1, grid=(r // tr,),
            in_specs=[pl.BlockSpec((tr, c), lambda i, core: (i, 0)),
                      pl.BlockSpec((len(CHIP_FLIPS), tr, c), lambda i, core: (0, i % per_half, 0))],
            out_specs=pl.BlockSpec((tr, c), lambda i, core: (i, 0))),
        out_shape=jax.ShapeDtypeStruct((r, c), F32), compiler_params=_params("parallel"),
    )(core, own, recv)


def _adamw_math(w, g, m, v):
    m = ADAM_B1 * m + (1.0 - ADAM_B1) * g
    v = ADAM_B2 * v + (1.0 - ADAM_B2) * (g * g)
    m_hat = m / (1.0 - ADAM_B1 ** ADAM_STEP)
    v_hat = v / (1.0 - ADAM_B2 ** ADAM_STEP)
    delta = -ADAM_LR * (m_hat / (jnp.sqrt(v_hat) + ADAM_EPS) + ADAM_WD * w)
    return delta, m, v


def _adamw(w, m, v, g_parts, name):
    r, c = w.shape
    tr = _row_tile(r)
    n = len(g_parts)

    def body(*refs):
        w_ref, m_ref, v_ref = refs[:3]
        g_refs = refs[3:3 + n]
        g_out, d_out, m_out, v_out = refs[3 + n:]
        g = g_refs[0][...]
        for ref in g_refs[1:]:
            g = g + ref[...]
        g_out[...] = g
        d_out[...], m_out[...], v_out[...] = _adamw_math(w_ref[...], g, m_ref[...], v_ref[...])

    blk = pl.BlockSpec((tr, c), lambda i: (i, 0))
    return pl.pallas_call(
        body, name=name, grid=(r // tr,), in_specs=[blk] * (3 + n), out_specs=[blk] * 4,
        out_shape=[jax.ShapeDtypeStruct((r, c), F32)] * 4, compiler_params=_params("parallel"),
    )(w, m, v, *g_parts)


def _adamw_small(total, w, m, v):
    sizes = [w[n].size for n in SMALL]
    flat = lambda d: [d[n].reshape(1, -1) for n in SMALL]
    k = len(SMALL)

    def body(*refs):
        total_ref, w_refs, m_refs, v_refs = refs[0], refs[1:1 + k], refs[1 + k:1 + 2 * k], refs[1 + 2 * k:1 + 3 * k]
        outs = refs[1 + 3 * k:]
        for i, size in enumerate(sizes):
            g = total_ref[i:i + 1, 0:size]
            outs[i][...] = g
            outs[k + i][...], outs[2 * k + i][...], outs[3 * k + i][...] = _adamw_math(
                w_refs[i][...], g, m_refs[i][...], v_refs[i][...])

    res = pl.pallas_call(
        body, name="adamw_small", out_shape=[jax.ShapeDtypeStruct((1, size), F32) for size in sizes] * 4,
        compiler_params=_params(),
    )(total, *flat(w), *flat(m), *flat(v))
    return [{n: res[j * k + i].reshape(w[n].shape) for i, n in enumerate(SMALL)} for j in range(4)]


SHARDED = ("w_in", "w_lora_up", "a_lora_up", "w_out_a", "w_out_b", "w_out")
ROW_SHARDED = ("w_out",)
SMALL = ("norm_g", "shift_mu", "w0", "a0", "k_k", "k_a", "r_k", "lnx_w", "lnx_b", "f_bias", "q_norm_g", "k_norm_g",
         "final_norm_g")
WEIGHTS = ("norm_g", "w_in", "shift_mu", "w_lora_up", "w0", "a_lora_up", "a0", "k_k", "k_a", "r_k", "lnx_w", "lnx_b",
           "f_bias", "q_norm_g", "k_norm_g", "w_out_a", "w_out_b", "w_out", "final_norm_g")
SLAB_ROWS = 16
SLAB_COLS = SEC


def _to_slab(named, extra=None):
    rows = [jnp.pad(named[n].reshape(1, -1), ((0, 0), (0, SLAB_COLS - named[n].size))) for n in SMALL]
    if extra is not None:
        rows.append(jnp.pad(extra.reshape(1, -1), ((0, 0), (0, SLAB_COLS - extra.size))))
    rows.append(jnp.zeros((SLAB_ROWS - len(rows), SLAB_COLS), F32))
    return jnp.concatenate(rows, axis=0)


def _by_chip(g, name):
    if name in ROW_SHARDED:
        return g.reshape(N_CHIPS, g.shape[0] // N_CHIPS, g.shape[1])
    r, c = g.shape
    return g.reshape(r, N_CHIPS, c // N_CHIPS).transpose(1, 0, 2)


def _from_chips(stack, name):
    if name in ROW_SHARDED:
        return stack.reshape(-1, stack.shape[2])
    _, r, c = stack.shape
    return stack.transpose(1, 0, 2).reshape(r, N_CHIPS * c)


def kernel(x, norm_g, w_in, shift_mu, w_lora_up, w0, a_lora_up, a0, k_k, k_a, r_k, lnx_w, lnx_b, f_bias, q_norm_g, k_norm_g, w_out_a, w_out_b, w_out, final_norm_g, loss_target, m_norm_g, m_w_in, m_shift_mu, m_w_lora_up, m_w0, m_a_lora_up, m_a0, m_k_k, m_k_a, m_r_k, m_lnx_w, m_lnx_b, m_f_bias, m_q_norm_g, m_k_norm_g, m_w_out_a, m_w_out_b, m_w_out, m_final_norm_g, v_norm_g, v_w_in, v_shift_mu, v_w_lora_up, v_w0, v_a_lora_up, v_a0, v_k_k, v_k_a, v_r_k, v_lnx_w, v_lnx_b, v_f_bias, v_q_norm_g, v_k_norm_g, v_w_out_a, v_w_out_b, v_w_out, v_final_norm_g):
    w = dict(norm_g=norm_g, w_in=w_in, shift_mu=shift_mu, w_lora_up=w_lora_up, w0=w0, a_lora_up=a_lora_up, a0=a0,
             k_k=k_k, k_a=k_a, r_k=r_k, lnx_w=lnx_w, lnx_b=lnx_b, f_bias=f_bias, q_norm_g=q_norm_g,
             k_norm_g=k_norm_g, w_out_a=w_out_a, w_out_b=w_out_b, w_out=w_out, final_norm_g=final_norm_g)
    m = dict(norm_g=m_norm_g, w_in=m_w_in, shift_mu=m_shift_mu, w_lora_up=m_w_lora_up, w0=m_w0,
             a_lora_up=m_a_lora_up, a0=m_a0, k_k=m_k_k, k_a=m_k_a, r_k=m_r_k, lnx_w=m_lnx_w, lnx_b=m_lnx_b,
             f_bias=m_f_bias, q_norm_g=m_q_norm_g, k_norm_g=m_k_norm_g, w_out_a=m_w_out_a, w_out_b=m_w_out_b,
             w_out=m_w_out, final_norm_g=m_final_norm_g)
    v = dict(norm_g=v_norm_g, w_in=v_w_in, shift_mu=v_shift_mu, w_lora_up=v_w_lora_up, w0=v_w0,
             a_lora_up=v_a_lora_up, a0=v_a0, k_k=v_k_k, k_a=v_k_a, r_k=v_r_k, lnx_w=v_lnx_w, lnx_b=v_lnx_b,
             f_bias=v_f_bias, q_norm_g=v_q_norm_g, k_norm_g=v_k_norm_g, w_out_a=v_w_out_a, w_out_b=v_w_out_b,
             w_out=v_w_out, final_norm_g=v_final_norm_g)
    shapes = {n: w[n].shape for n in WEIGHTS}

    shard = {n: w[n][0].astype(BF16) for n in SHARDED}
    late = ("w_out_a", "w_out_b", "w_out")
    loras = ("w_lora_up", "a_lora_up")
    w_in_head, w_in_tail = shard["w_in"][:, :A_TAIL], shard["w_in"][:, A_TAIL:]
    shard0, shard1_head, up_stack, aup_stack = _run_on_sequencer(_gather_exchange(
        [(0, shard["w_in"]), (1, w_in_head)], [shard[n] for n in loras], split=(0, 1)), "gather_early", 2)
    moments = (_row_major_copy(m["w_in"][0], "m_w_in_rows"), _row_major_copy(v["w_in"][0], "v_w_in_rows"))
    shard0, moments = lax.optimization_barrier((shard0, moments))
    w_a = jnp.concatenate([shard0, shard1_head], axis=1)

    def late_weights(arrived):
        shard1_tail, shard2, shard3 = arrived[:3]
        w_b = jnp.concatenate([shard1_tail, shard2[:, :B_TAIL], jnp.zeros((D_MODEL, SEC - FOX_REAL), BF16)], axis=1)
        w_g = jnp.concatenate([shard2[:, B_TAIL:], shard3], axis=1)
        return (w_b, w_g, *[_from_chips(s, n) for n, s in zip(late, arrived[3:])])

    own = {}
    cut = {"block0": lambda: own["dw_a"][:, :SHARD_COLS], "head1": lambda: own["dw_a"][:, SHARD_COLS:],
           "tail1": lambda: own["dw_b"][:, :B_HEAD],
           "block2": lambda: jnp.concatenate([own["dw_b"][:, B_HEAD:FOX_REAL], own["dw_g"][:, :G_HEAD]], axis=1),
           "block3": lambda: own["dw_g"][:, G_HEAD:]}

    def bwd_exchange(dw_b, dw_g, dwa, dwb, dwo):
        own.update(dw_b=dw_b, dw_g=dw_g)
        own.update({n: _by_chip(g, n) for n, g in zip(late, (dwa, dwb, dwo))})
        return _scatter_exchange([(1, cut["tail1"]().astype(BF16)), (2, cut["block2"]().astype(BF16)),
                                  (3, cut["block3"]().astype(BF16))], [own[n].astype(BF16) for n in late])

    def tail_exchange(dw_a, dw_up, da_up):
        own.update(dw_a=dw_a)
        own.update({n: _by_chip(g, n) for n, g in zip(loras, (dw_up, da_up))})
        pair = _pair_halves(dw_a)
        return _scatter_exchange([(0, pair[:, :SHARD_COLS]), (1, pair[:, SHARD_COLS:])],
                                 [own[n].astype(BF16) for n in loras], via_neighbours=True)

    small = {n: w[n] for n in SMALL}
    loss_vec, grad_x, grads, sent, sent_last = _device_grads(
        x[0], loss_target[0], small, w_a, _from_chips(up_stack, "w_lora_up"), _from_chips(aup_stack, "a_lora_up"),
        late_weights, _gather_exchange([(1, w_in_tail), (2, shard["w_in"]), (3, shard["w_in"])], [shard[n] for n in late]),
        bwd_exchange, tail_exchange)

    total = _allreduce_small(_to_slab(grads, extra=loss_vec))
    loss = (0.5 / D_MODEL) * jnp.sum(total[len(SMALL)])
    out_g, out_d, out_m, out_v = _adamw_small(total, w, m, v)

    xpos, ypos, cpos = _position()
    me = (2 * xpos + ypos).astype(jnp.int32).reshape(1)
    core = cpos.astype(jnp.int32).reshape(1)
    core_sum, theirs = {}, {}

    def update(n):
        m_n, v_n = moments if n == "w_in" else (m[n][0], v[n][0])
        g, d, m2, v2 = _adamw(w[n][0], m_n, v_n, [core_sum[n], theirs[n]], "adamw_" + n)
        out_g[n], out_d[n], out_m[n], out_v[n] = (a.reshape(shapes[n]) for a in (g, d, m2, v2))

    sent_last, out_d["norm_g"] = lax.optimization_barrier((sent_last, out_d["norm_g"]))
    core_sum["w_in"] = lax.switch(me[0], [
        lambda: _sum_half(cut["block0"](), sent_last[0], core),
        lambda: jnp.concatenate([_sum_half(cut["head1"](), sent_last[1], core), _sum_block(cut["tail1"](), sent[0])],
                                axis=1),
        lambda: _sum_block(cut["block2"](), sent[1]),
        lambda: _sum_block(cut["block3"](), sent[2])])
    core_sum.update({n: _sum4(own[n], r, me) for n, r in zip(loras, sent_last[2:])})
    rest = ("w_in",) + loras
    theirs.update(zip(rest, _swap_sibling([core_sum[n] for n in rest], "swap_sibling")))
    for n in rest:
        update(n)
    out_d["w_in"], sent_late = lax.optimization_barrier((out_d["w_in"], sent[3:]))
    core_sum.update({n: _sum4(own[n], r, me) for n, r in zip(late, sent_late)})
    theirs.update(zip(late, _swap_sibling([core_sum[n] for n in late], "swap_sibling_late")))
    for n in late:
        update(n)

    return (loss, grad_x.reshape(x.shape), *[out_g[n] for n in WEIGHTS], *[out_d[n] for n in WEIGHTS],
            *[out_m[n] for n in WEIGHTS], *[out_v[n] for n in WEIGHTS])
```

```python
import functools
import math

import jax
import jax.numpy as jnp
from jax import lax
from jax.experimental import pallas as pl
from jax.experimental.pallas import tpu as pltpu
from jax.experimental.pallas import tpu_sc as plsc

F32 = jnp.float32
BF16 = jnp.bfloat16

D_MODEL = 1024
D_HALF = 512
HEAD = 64
N_HEADS = 8
LORA = 64
RWKV_COLS = 2176
FOX_REAL = 2056
SEC = 2176
GATE_COLS = 2048
IN_COLS = 6280
N_CHIPS = 4
SHARD_COLS = IN_COLS // N_CHIPS
A_TAIL = RWKV_COLS - SHARD_COLS
B_HEAD = SHARD_COLS - A_TAIL
B_TAIL = FOX_REAL - B_HEAD
G_HEAD = SHARD_COLS - B_TAIL
RMS_EPS = 1e-6
LNX_EPS = 64e-5
ATT_SCALE = HEAD ** -0.5
NEG = -1e30

ADAM_LR = 0.001
ADAM_B1 = 0.9
ADAM_B2 = 0.999
ADAM_EPS = 1e-08
ADAM_WD = 0.01
ADAM_STEP = 10

LANES = 128
SUBLANES = 8
VMEM_LIMIT = 56 * 1024 * 1024
MESH = pl.DeviceIdType.MESH


def _params(*sem):
    return pltpu.CompilerParams(dimension_semantics=sem if sem else None, vmem_limit_bytes=VMEM_LIMIT)


def _sigmoid(x):
    return 1.0 / (1.0 + jnp.exp(-x))


def _log_sigmoid(x):
    return jnp.minimum(x, 0.0) - jnp.log(1.0 + jnp.exp(-jnp.abs(x)))


def _head_ones():
    r = lax.broadcasted_iota(jnp.int32, (LANES, LANES), 0) >> 6
    c = lax.broadcasted_iota(jnp.int32, (LANES, LANES), 1) >> 6
    return (r == c).astype(BF16)


def _split3(x):
    hi = x.astype(BF16)
    r1 = x - hi.astype(F32)
    mid = r1.astype(BF16)
    lo = (r1 - mid.astype(F32)).astype(BF16)
    return hi, mid, lo


def _exact_dot(x, ones_bf16, ones_first=False):
    out = None
    for piece in _split3(x):
        if ones_first:
            t = jnp.dot(ones_bf16, piece, preferred_element_type=F32)
        else:
            t = jnp.dot(piece, ones_bf16, preferred_element_type=F32)
        out = t if out is None else out + t
    return out


def _head_sum(x, bd):
    n = x.shape[1] // LANES
    parts = [_exact_dot(x[:, i * LANES:(i + 1) * LANES], bd) for i in range(n)]
    return parts[0] if n == 1 else jnp.concatenate(parts, axis=1)


def _dot_nt(a, b):
    return lax.dot_general(a, b, (((1,), (1,)), ((), ())), preferred_element_type=F32)


def _dot_tn(a, b):
    return lax.dot_general(a, b, (((0,), (0,)), ((), ())), preferred_element_type=F32)


def _colsum(x):
    return jnp.sum(x, axis=0, keepdims=True)


def _matmul_tn_acc(at, b, name, tk=512):
    m, k = at.shape
    n = b.shape[1]

    def body(a_ref, b_ref, o_ref):
        j = pl.program_id(0)

        @pl.when(j == 0)
        def _():
            o_ref[...] = jnp.zeros_like(o_ref)

        o_ref[...] += jnp.dot(a_ref[...], b_ref[...].astype(BF16), preferred_element_type=F32)

    return pl.pallas_call(
        body, name=name, grid=(k // tk,),
        in_specs=[pl.BlockSpec((m, tk), lambda j: (0, j)), pl.BlockSpec((tk, n), lambda j: (j, 0))],
        out_specs=pl.BlockSpec((m, n), lambda j: (0, 0)),
        out_shape=jax.ShapeDtypeStruct((m, n), F32), compiler_params=_params("arbitrary"),
    )(at, b)


def _inproj_bwd(du_a, du_b, du_g, w_a, w_b, w_g, x, dx2, g, exchange=None, tm=256):
    s, d = x.shape
    nb = s // tm

    def body(*refs):
        ((da_ref, db_ref, dg_ref, wa_ref, wb_ref, wg_ref, x_ref, dx2_ref, g_ref), (gx_ref, gg_ref), _,
         moves) = _split_refs(refs, 9, 2, exchange)
        i = pl.program_id(0)
        if moves:
            moves.start(also=(i == 0))

        @pl.when(i == 0)
        def _():
            gg_ref[...] = jnp.zeros_like(gg_ref)

        dh = _dot_nt(da_ref[...].astype(BF16), wa_ref[...])
        dh += _dot_nt(db_ref[...].astype(BF16), wb_ref[...])
        dh += _dot_nt(dg_ref[...].astype(BF16), wg_ref[...])
        xv = x_ref[...]
        r = lax.rsqrt(jnp.mean(xv * xv, axis=-1, keepdims=True) + RMS_EPS)
        xh = xv * r
        gg_ref[...] += _colsum(dh * xh)
        dxh = dh * g_ref[...]
        gx_ref[...] = dx2_ref[...] + r * (dxh - xh * jnp.mean(dxh * xh, axis=-1, keepdims=True))
        if moves:
            moves.wait(also=(i == nb - 1))

    row = lambda w: pl.BlockSpec((tm, w), lambda i: (i, 0))
    full = lambda a: pl.BlockSpec(a.shape, lambda i: (0, 0))
    ex_in = exchange.operands if exchange else []
    ex_out = exchange.out_shapes if exchange else []
    res = pl.pallas_call(
        body, name="inproj_bwd", grid=(nb,),
        in_specs=[row(SEC), row(SEC), row(GATE_COLS), full(w_a), full(w_b), full(w_g), row(d), row(d), full(g)]
                 + [ANY] * len(ex_in),
        out_specs=[row(d), pl.BlockSpec((1, d), lambda i: (0, 0))] + [ANY] * len(ex_out),
        out_shape=[jax.ShapeDtypeStruct((s, d), F32), jax.ShapeDtypeStruct((1, d), F32)] + ex_out,
        scratch_shapes=exchange.scratch() if exchange else [],
        compiler_params=_params("arbitrary"),
    )(du_a, du_b, du_g, w_a, w_b, w_g, x, dx2, g, *ex_in)
    return res[0], res[1], list(res[2:])


def _rwkv_elementwise(ua, prev_row, first, mu, wl, w0, a0, kkw, kaw, bd):
    tm = ua.shape[0]
    rows = lax.broadcasted_iota(jnp.int32, (tm, 1), 0)
    prev = jnp.where(first, jnp.zeros_like(prev_row), prev_row)
    shifted = jnp.where(rows == 0, prev, pltpu.roll(ua, 1, 0))
    delta = shifted - ua
    us = ua + delta * mu
    r = us[:, 0:512]
    k0 = us[:, 512:1024]
    v = us[:, 1024:1536]
    lo = us[:, 1536:1664]
    gate = us[:, 1664:2176]
    lane = lax.broadcasted_iota(jnp.int32, (1, LANES), 1)
    th = jnp.tanh(lo)
    lin = jnp.where(lane < LORA, th, lo)
    ll = jnp.dot(lin.astype(BF16), wl, preferred_element_type=F32)
    sz = _sigmoid(w0 + ll[:, :512])
    e = sz * math.exp(-0.5)
    dec = jnp.exp(-e)
    a = _sigmoid(a0 + ll[:, 512:])
    kk0 = k0 * kkw
    ss = _head_sum(kk0 * kk0, bd)
    nrm = jnp.maximum(jnp.sqrt(ss), 1e-12)
    kk = kk0 / nrm
    k = k0 * (1.0 + (a - 1.0) * kaw)
    return dict(delta=delta, us=us, r=r, k0=k0, v=v, lo=lo, gate=gate, th=th, lin=lin, sz=sz, e=e, dec=dec,
                a=a, kk0=kk0, ss=ss, nrm=nrm, kk=kk, k=k)


def _rwkv_front(x, g, w_a, mu, wl, w0, a0, kkw, kaw, tm=256):
    s, d = x.shape

    def body(x_ref, g_ref, wa_ref, mu_ref, wl_ref, w0_ref, a0_ref, kkw_ref, kaw_ref,
             h_ref, ua_ref, r_ref, w_ref, k_ref, v_ref, a_ref, b_ref, gate_ref, last_row):
        i = pl.program_id(0)
        xv = x_ref[...]
        h = (xv * lax.rsqrt(jnp.mean(xv * xv, axis=-1, keepdims=True) + RMS_EPS) * g_ref[...]).astype(BF16)
        h_ref[...] = h
        ua = jnp.dot(h, wa_ref[...], preferred_element_type=F32)
        ua_ref[...] = ua

        @pl.when(i == 0)
        def _():
            last_row[...] = jnp.zeros_like(last_row)

        f = _rwkv_elementwise(ua, last_row[...], i == 0, mu_ref[...], wl_ref[...], w0_ref[...],
                              a0_ref[...], kkw_ref[...], kaw_ref[...], _head_ones())
        last_row[...] = ua[tm - 1:tm, :]
        r_ref[...] = f["r"]
        w_ref[...] = f["dec"]
        k_ref[...] = f["k"]
        v_ref[...] = f["v"]
        a_ref[...] = -f["kk"]
        b_ref[...] = f["kk"] * f["a"]
        gate_ref[...] = f["gate"]

    vec = lambda w: pl.BlockSpec((1, w), lambda i: (0, 0))
    row = lambda w: pl.BlockSpec((tm, w), lambda i: (i, 0))
    return pl.pallas_call(
        body, name="rwkv_front", grid=(s // tm,),
        in_specs=[row(d), vec(d), pl.BlockSpec(w_a.shape, lambda i: (0, 0), pipeline_mode=pl.Buffered(1)),
                  vec(SEC), pl.BlockSpec((LANES, 2 * D_HALF), lambda i: (0, 0)),
                  vec(D_HALF), vec(D_HALF), vec(D_HALF), vec(D_HALF)],
        out_specs=[row(d), row(SEC)] + [row(D_HALF)] * 7,
        out_shape=[jax.ShapeDtypeStruct((s, d), BF16), jax.ShapeDtypeStruct((s, SEC), F32)]
                  + [jax.ShapeDtypeStruct((s, D_HALF), F32)] * 7,
        scratch_shapes=[pltpu.VMEM((1, SEC), F32)],
        compiler_params=_params("arbitrary"),
    )(x, g, w_a, mu, wl, w0, a0, kkw, kaw)


SCAN_TB = 128
N_PAIRS = 4


def _pair_sum(x, left):
    s_l = jnp.sum(jnp.where(left, x, 0.0), axis=1, keepdims=True)
    s_r = jnp.sum(jnp.where(left, 0.0, x), axis=1, keepdims=True)
    return jnp.where(left, s_l, s_r)


def _pair_dot(x, row_l, row_r, left):
    s_l = jnp.sum(x * row_l, axis=1, keepdims=True)
    s_r = jnp.sum(x * row_r, axis=1, keepdims=True)
    return jnp.where(left, s_l, s_r)


def _halves(rows8):
    lane = lax.broadcasted_iota(jnp.int32, rows8.shape, 1)
    keep_left = (lane & (LANES - 1)) < HEAD
    return jnp.where(keep_left, rows8, 0.0), jnp.where(keep_left, 0.0, rows8)


def _quad_consts():
    lane = lax.broadcasted_iota(jnp.int32, (HEAD, 2 * LANES), 1)
    rowi = lax.broadcasted_iota(jnp.int32, (HEAD, 2 * LANES), 0)
    diag2 = rowi == (lane & (HEAD - 1))
    r = lax.broadcasted_iota(jnp.int32, (2 * LANES, 2 * LANES), 0) >> 6
    c = lax.broadcasted_iota(jnp.int32, (2 * LANES, 2 * LANES), 1) >> 6
    return diag2, (r == c).astype(BF16)


def _rows_to_columns(x8, diag2, bd2):
    lhs = jnp.concatenate([jnp.where(diag2, x8[i:i + 1], 0.0).astype(BF16) for i in range(SUBLANES)], axis=0)
    return jnp.dot(lhs, bd2, preferred_element_type=F32)


def _diag_rows(qtile, diag2, bd2, sub_row2):
    res = jnp.dot(qtile, bd2, preferred_element_type=F32)
    out = jnp.zeros((SUBLANES, 2 * LANES), F32)
    for i in range(SUBLANES):
        out = jnp.where(sub_row2 == i, _colsum(jnp.where(diag2, res[i * HEAD:(i + 1) * HEAD], 0.0)), out)
    return out


def _store_tile(qbuf, slot, p, i, x):
    qbuf[slot, p // 2, i * HEAD:(i + 1) * HEAD, (p % 2) * LANES:(p % 2 + 1) * LANES] = x.astype(BF16)


def _left_half():
    return lax.broadcasted_iota(jnp.int32, (HEAD, LANES), 1) < HEAD


def _split_refs(refs, n_rows, n_out, exchange):
    n_in = len(exchange.operands) if exchange else 0
    n_ex_out = len(exchange.out_shapes) if exchange else 0
    refs = list(refs)
    rows, refs = refs[:n_rows], refs[n_rows:]
    ex_in, refs = refs[:n_in], refs[n_in:]
    outs, refs = refs[:n_out], refs[n_out:]
    ex_out, refs = refs[:n_ex_out], refs[n_ex_out:]
    scratch, sems = (refs[:-3], refs[-3:]) if exchange else (refs, None)
    moves = exchange.moves(ex_in, ex_out, sems) if exchange else None
    return rows, outs, scratch, moves


def _wkv_fwd(r, w, k, a, b, v, exchange=None):
    s = r.shape[0]
    tb = SCAN_TB
    nb = s // tb

    def body(*refs):
        (r_ref, w_ref, k_ref, a_ref, b_ref, v_ref), (y_ref, st_ref), (state, vbuf, qbuf), moves = _split_refs(
            refs, 6, 2, exchange)
        g = pl.program_id(0)
        if moves:
            moves.start(also=(g == 0))

        @pl.when(g == 0)
        def _():
            state[...] = jnp.zeros_like(state)
            qbuf[...] = jnp.zeros_like(qbuf)

        left = _left_half()
        diag2, bd2 = _quad_consts()
        sub_row2 = lax.broadcasted_iota(jnp.int32, (SUBLANES, 2 * LANES), 0)
        groups = tb // SUBLANES
        quads = [slice(g2 * 2 * LANES, (g2 + 1) * 2 * LANES) for g2 in range(2)]

        def rows_of(q):
            return pl.ds(pl.multiple_of(q * SUBLANES, SUBLANES), SUBLANES)

        def v_tiles(q, slot):
            v8 = v_ref[rows_of(q), :]
            for g2 in range(2):
                vbuf[slot, g2] = _rows_to_columns(v8[:, quads[g2]], diag2, bd2)

        def chain(q, slot):
            rows8 = rows_of(q)
            a8, w8, b8, k8, r8 = (x[rows8, :] for x in (a_ref, w_ref, b_ref, k_ref, r_ref))
            pairs = [slice(p * LANES, (p + 1) * LANES) for p in range(N_PAIRS)]
            a_next = pltpu.roll(a8, SUBLANES - 1, 0)
            (a8_l, a8_r), (wa8_l, wa8_r) = _halves(a8), _halves(w8 * a_next)
            ba8 =jnp.concatenate([_pair_sum(b8[:, pr] * a_next[:, pr], left[0:SUBLANES]) for pr in pairs], axis=1)
            ka8 = jnp.concatenate([_pair_sum(k8[:, pr] * a_next[:, pr], left[0:SUBLANES]) for pr in pairs], axis=1)
            sp = [state[p] for p in range(N_PAIRS)]
            for i in range(0, SUBLANES, 2):
                r0, r1 = slice(i, i + 1), slice(i + 1, i + 2)
                sums = [(_pair_dot(sp[p], a8_l[r0, pairs[p]], a8_r[r0, pairs[p]], left),
                         _pair_dot(sp[p], wa8_l[r0, pairs[p]], wa8_r[r0, pairs[p]], left)) for p in range(N_PAIRS)]
                sa0, sa1 = [s[0] for s in sums], [s[1] for s in sums]
                for p in range(N_PAIRS):
                    pr = pairs[p]
                    inner = slice((p % 2) * LANES, (p % 2 + 1) * LANES)
                    vt0 = vbuf[slot, p // 2, i * HEAD:(i + 1) * HEAD, inner]
                    vt1 = vbuf[slot, p // 2, (i + 1) * HEAD:(i + 2) * HEAD, inner]
                    sa_next = sa1[p] + sa0[p] * ba8[r0, pr] + vt0 * ka8[r0, pr]
                    s1 = sp[p] * w8[r0, pr] + sa0[p] * b8[r0, pr] + vt0 * k8[r0, pr]
                    st_ref[q * SUBLANES + i, p] = s1
                    _store_tile(qbuf, slot, p, i, s1 * r8[r0, pr])
                    s2 = s1 * w8[r1, pr] + sa_next * b8[r1, pr] + vt1 * k8[r1, pr]
                    st_ref[q * SUBLANES + i + 1, p] = s2
                    _store_tile(qbuf, slot, p, i + 1, s2 * r8[r1, pr])
                    sp[p] = s2
            for p in range(N_PAIRS):
                state[p] = sp[p]

        def y_rows(q, slot):
            for g2 in range(2):
                y_ref[rows_of(q), quads[g2]] = _diag_rows(qbuf[slot, g2], diag2, bd2, sub_row2)

        v_tiles(0, 0)

        def two_groups(j, carry):
            q0 = 2 * j
            v_tiles(q0 + 1, 1)
            chain(q0, 0)
            y_rows(jnp.maximum(q0 - 1, 0), 1)
            v_tiles(jnp.minimum(q0 + 2, groups - 1), 0)
            chain(q0 + 1, 1)
            y_rows(q0, 0)
            return carry

        lax.fori_loop(0, groups // 2, two_groups, 0)
        y_rows(groups - 1, 1)
        if moves:
            moves.wait(also=(g == nb - 1))

    rows = pl.BlockSpec((tb, D_HALF), lambda g: (g, 0))
    ex_in = exchange.operands if exchange else []
    ex_out = exchange.out_shapes if exchange else []
    res = pl.pallas_call(
        body, name="wkv_fwd", grid=(nb,),
        in_specs=[rows] * 6 + [ANY] * len(ex_in),
        out_specs=[rows, pl.BlockSpec((tb, N_PAIRS, HEAD, LANES), lambda g: (g, 0, 0, 0))] + [ANY] * len(ex_out),
        out_shape=[jax.ShapeDtypeStruct((s, D_HALF), F32),
                   jax.ShapeDtypeStruct((s, N_PAIRS, HEAD, LANES), F32)] + ex_out,
        scratch_shapes=[pltpu.VMEM((N_PAIRS, HEAD, LANES), F32),
                        pltpu.VMEM((2, 2, SUBLANES * HEAD, 2 * LANES), F32),
                        pltpu.VMEM((2, 2, SUBLANES * HEAD, 2 * LANES), BF16)]
                       + (exchange.scratch() if exchange else []),
        compiler_params=_params("arbitrary"),
    )(r, w, k, a, b, v, *ex_in)
    return res[0], res[1], list(res[2:])


def _wkv_bwd(r, w, k, a, b, v, dy, st, exchange=None):
    s = r.shape[0]
    tb = SCAN_TB
    nb = s // tb

    def body(*refs):
        ((r_ref, w_ref, k_ref, a_ref, b_ref, v_ref, dy_ref, st_ref, before_ref),
         (dr_ref, dw_ref, dk_ref, dv_ref, da_ref, db_ref), (dstate, vbuf, qbuf, sbuf),
         moves) = _split_refs(refs, 9, 6, exchange)
        g = pl.program_id(0)
        first_block = g == nb - 1
        if moves:
            moves.start(also=(g == 0))

        @pl.when(g == 0)
        def _():
            dstate[...] = jnp.zeros_like(dstate)
            qbuf[...] = jnp.zeros_like(qbuf)

        left = _left_half()
        diag2, bd2 = _quad_consts()
        sub_row = lax.broadcasted_iota(jnp.int32, (SUBLANES, LANES), 0)
        sub_row2 = lax.broadcasted_iota(jnp.int32, (SUBLANES, 2 * LANES), 0)
        groups = tb // SUBLANES
        quads = [slice(g2 * 2 * LANES, (g2 + 1) * 2 * LANES) for g2 in range(2)]
        row_refs = (dr_ref, dw_ref, dk_ref, da_ref, db_ref)

        def rows_of(q):
            return pl.ds(pl.multiple_of(q * SUBLANES, SUBLANES), SUBLANES)

        def state_before(q, i, p):
            if i > 0:
                return st_ref[q * SUBLANES + i - 1, p]
            return jnp.where(q == 0, jnp.where(first_block, 0.0, before_ref[0, p]),
                             st_ref[jnp.maximum(q * SUBLANES - 1, 0), p])

        def column_tiles(q, slot):
            rows8 = rows_of(q)
            for kind, ref in enumerate((v_ref, dy_ref)):
                x8 = ref[rows8, :]
                for g2 in range(2):
                    vbuf[slot, kind, g2] = _rows_to_columns(x8[:, quads[g2]], diag2, bd2)
            a8 = a_ref[rows8, :]
            for i in range(SUBLANES):
                for p in range(N_PAIRS):
                    _store_tile(sbuf, 0, p, i, state_before(q, i, p) * a8[i:i + 1, p * LANES:(p + 1) * LANES])
            for g2 in range(2):
                vbuf[slot, 2, g2] = jnp.dot(sbuf[0, g2], bd2, preferred_element_type=F32)

        def chain(q, slot):
            rows8 = rows_of(q)
            a8, w8, b8, k8, r8 = (x[rows8, :] for x in (a_ref, w_ref, b_ref, k_ref, r_ref))
            b8_l, b8_r = _halves(b8)
            dsp = [dstate[p] for p in range(N_PAIRS)]
            outs = [[jnp.zeros((SUBLANES, LANES), F32) for _ in row_refs] for _ in range(N_PAIRS)]
            after = [st_ref[q * SUBLANES + SUBLANES - 1, p] for p in range(N_PAIRS)]
            for i in reversed(range(SUBLANES)):
                row = slice(i, i + 1)
                pl_ = [slice(p * LANES, (p + 1) * LANES) for p in range(N_PAIRS)]
                tile = [(p // 2, slice(i * HEAD, (i + 1) * HEAD), slice((p % 2) * LANES, (p % 2 + 1) * LANES))
                        for p in range(N_PAIRS)]
                sp = [state_before(q, i, p) for p in range(N_PAIRS)]
                dyt = [vbuf[(slot, 1) + tile[p]] for p in range(N_PAIRS)]
                ds = [dsp[p] + dyt[p] * r8[row, pl_[p]] for p in range(N_PAIRS)]
                dsa = [_pair_dot(ds[p], b8_l[row, pl_[p]], b8_r[row, pl_[p]], left) for p in range(N_PAIRS)]
                sa = [vbuf[(slot, 2) + tile[p]] for p in range(N_PAIRS)]
                for p in range(N_PAIRS):
                    ar, wr, br, kr = (x[row, pl_[p]] for x in (a8, w8, b8, k8))
                    vt = vbuf[(slot, 0) + tile[p]]
                    dsp[p] = ds[p] * wr + dsa[p] * ar
                    new = (_colsum(after[p] * dyt[p]), _colsum(ds[p] * sp[p]), _colsum(ds[p] * vt),
                           _colsum(sp[p] * dsa[p]), _colsum(ds[p] * sa[p]))
                    outs[p] = [jnp.where(sub_row == i, n, o) for n, o in zip(new, outs[p])]
                    _store_tile(qbuf, slot, p, i, ds[p] * kr)
                after = sp
            for p in range(N_PAIRS):
                dstate[p] = dsp[p]
                for ref, o in zip(row_refs, outs[p]):
                    ref[rows8, p * LANES:(p + 1) * LANES] = o

        def dv_rows(q, slot):
            for g2 in range(2):
                dv_ref[rows_of(q), quads[g2]] = _diag_rows(qbuf[slot, g2], diag2, bd2, sub_row2)

        column_tiles(groups - 1, 0)

        def two_groups(j, carry):
            q0 = groups - 1 - 2 * j
            column_tiles(q0 - 1, 1)
            chain(q0, 0)
            dv_rows(jnp.minimum(q0 + 1, groups - 1), 1)
            column_tiles(jnp.maximum(q0 - 2, 0), 0)
            chain(q0 - 1, 1)
            dv_rows(q0, 0)
            return carry

        lax.fori_loop(0, groups // 2, two_groups, 0)
        dv_rows(0, 1)
        if moves:
            moves.wait(also=(g == nb - 1))

    rows = pl.BlockSpec((tb, D_HALF), lambda g: (nb - 1 - g, 0))
    ex_in = exchange.operands if exchange else []
    ex_out = exchange.out_shapes if exchange else []
    res = pl.pallas_call(
        body, name="wkv_bwd", grid=(nb,),
        in_specs=[rows] * 7 + [pl.BlockSpec((tb, N_PAIRS, HEAD, LANES), lambda g: (nb - 1 - g, 0, 0, 0)),
                               pl.BlockSpec((1, N_PAIRS, HEAD, LANES),
                                            lambda g: (jnp.maximum((nb - 1 - g) * tb - 1, 0), 0, 0, 0))]
                 + [ANY] * len(ex_in),
        out_specs=[rows] * 6 + [ANY] * len(ex_out),
        out_shape=[jax.ShapeDtypeStruct((s, D_HALF), F32)] * 6 + ex_out,
        scratch_shapes=[pltpu.VMEM((N_PAIRS, HEAD, LANES), F32),
                        pltpu.VMEM((2, 3, 2, SUBLANES * HEAD, 2 * LANES), F32),
                        pltpu.VMEM((2, 2, SUBLANES * HEAD, 2 * LANES), BF16),
                        pltpu.VMEM((1, 2, SUBLANES * HEAD, 2 * LANES), BF16)]
                       + (exchange.scratch() if exchange else []),
        compiler_params=_params("arbitrary"),
    )(r, w, k, a, b, v, dy, st, st, *ex_in)
    return list(res[:6]), list(res[6:])


def _rwkv_post_math(y, r, k, v, gate, lw, lb, rk, bd):
    mean = _head_sum(y, bd) * (1.0 / HEAD)
    yc = y - mean
    var = _head_sum(yc * yc, bd) * (1.0 / HEAD)
    rstd = lax.rsqrt(var + LNX_EPS)
    yn = yc * rstd
    rkk = _head_sum(r * k * rk, bd)
    sg = _sigmoid(gate)
    pre = yn * lw + lb + rkk * v
    return yn, rstd, rkk, sg, pre


def _rwkv_prep_bwd(u_a, h_t, grads, mu, wl, w0, a0, kkw, kaw, tm=256):
    s = u_a.shape[0]
    nb = s // tm
    d = h_t.shape[0]

    def body(ua_ref, prev_ref, ht_ref, drs_ref, dws_ref, dks_ref, dvs_ref, das_ref, dbs_ref, drb_ref, dkb_ref, dvb_ref,
             dgt_ref, mu_ref, wl_ref, w0_ref, a0_ref, kkw_ref, kaw_ref,
             du_ref, dwa_ref, dmu_ref, dwl_ref, dw0_ref, da0_ref, dkkw_ref, dkaw_ref, carry):
        i = pl.program_id(0)

        @pl.when(i == 0)
        def _():
            carry[...] = jnp.zeros_like(carry)
            for ref in (dwa_ref, dmu_ref, dwl_ref, dw0_ref, da0_ref, dkkw_ref, dkaw_ref):
                ref[...] = jnp.zeros_like(ref)

        bd = _head_ones()
        mu_v, wl_v, kkw_v, kaw_v = mu_ref[...], wl_ref[...], kkw_ref[...], kaw_ref[...]
        f = _rwkv_elementwise(ua_ref[...], prev_ref[7:8, :], i == nb - 1, mu_v, wl_v, w0_ref[...],
                              a0_ref[...], kkw_v, kaw_v, bd)
        a, kk, k0 = f["a"], f["kk"], f["k0"]
        dk = dks_ref[...] + dkb_ref[...]
        dbs = dbs_ref[...]
        dkk = dbs * a - das_ref[...]
        da = dbs * kk + dk * k0 * kaw_v
        dk0 = dk * (1.0 + (a - 1.0) * kaw_v)
        dkaw_ref[...] += _colsum(dk * k0 * (a - 1.0))
        inv = 1.0 / f["nrm"]
        proj = _head_sum(dkk * kk, bd)
        dkk0 = jnp.where(f["ss"] > 1e-24, (dkk - kk * proj) * inv, dkk * inv)
        dk0 = dk0 + dkk0 * kkw_v
        dkkw_ref[...] += _colsum(dkk0 * k0)
        dza = da * a * (1.0 - a)
        da0_ref[...] += _colsum(dza)
        dz = -dws_ref[...] * f["dec"] * f["e"] * (1.0 - f["sz"])
        dw0_ref[...] += _colsum(dz)
        dll = jnp.concatenate([dz, dza], axis=1).astype(BF16)
        dwl_ref[...] += _dot_tn(f["lin"].astype(BF16), dll)
        dlin = _dot_nt(dll, wl_v)
        lane = lax.broadcasted_iota(jnp.int32, (1, LANES), 1)
        th = f["th"]
        dlo = jnp.where(lane < LORA, dlin * (1.0 - th * th), dlin)
        dus = jnp.concatenate([drs_ref[...] + drb_ref[...], dk0, dvs_ref[...] + dvb_ref[...], dlo, dgt_ref[...]],
                              axis=1)
        dmu_ref[...] += _colsum(dus * f["delta"])
        g1 = dus * mu_v
        rows = lax.broadcasted_iota(jnp.int32, (tm, 1), 0)
        up = jnp.where(rows == tm - 1, carry[...], pltpu.roll(g1, tm - 1, 0))
        dua = dus - g1 + up
        du_ref[...] = dua
        dwa_ref[...] += jnp.dot(ht_ref[...], dua.astype(BF16), preferred_element_type=F32)
        carry[...] = g1[0:1, :]

    rev = lambda w: pl.BlockSpec((tm, w), lambda i: (nb - 1 - i, 0))
    vec = lambda w: pl.BlockSpec((1, w), lambda i: (0, 0))
    wl_spec = pl.BlockSpec((LANES, 2 * D_HALF), lambda i: (0, 0))
    return pl.pallas_call(
        body, name="rwkv_prep_bwd", grid=(nb,),
        in_specs=[rev(SEC), pl.BlockSpec((8, SEC), lambda i: (jnp.maximum((nb - 1 - i) * (tm // 8) - 1, 0), 0)),
                  pl.BlockSpec((d, tm), lambda i: (0, nb - 1 - i))]
                 + [rev(D_HALF)] * 10 + [vec(SEC), wl_spec] + [vec(D_HALF)] * 4,
        out_specs=[rev(SEC), pl.BlockSpec((d, SEC), lambda i: (0, 0)), vec(SEC), wl_spec] + [vec(D_HALF)] * 4,
        out_shape=[jax.ShapeDtypeStruct((s, SEC), F32), jax.ShapeDtypeStruct((d, SEC), F32),
                   jax.ShapeDtypeStruct((1, SEC), F32),
                   jax.ShapeDtypeStruct((LANES, 2 * D_HALF), F32)] + [jax.ShapeDtypeStruct((1, D_HALF), F32)] * 4,
        scratch_shapes=[pltpu.VMEM((1, SEC), F32)],
        compiler_params=_params("arbitrary"),
    )(u_a, u_a, h_t, *grads, mu, wl, w0, a0, kkw, kaw)


def _tri(tm, lower):
    r = lax.broadcasted_iota(jnp.int32, (tm, tm), 0)
    c = lax.broadcasted_iota(jnp.int32, (tm, tm), 1)
    return ((r >= c) if lower else (r <= c)).astype(BF16)


def _head_rms(x, g, bd):
    rinv = lax.rsqrt(_head_sum(x * x, bd) * (1.0 / HEAD) + RMS_EPS)
    xh = x * rinv
    return xh, rinv, xh * g


def _fox_front(h, w_b, fb, qg, kg, tm=256):
    s, d = h.shape

    def body(h_ref, wb_ref, fb_ref, qg_ref, kg_ref, ub_ref, q_ref, k_ref, v_ref, cc_ref, cr_ref, carry):
        i = pl.program_id(0)

        @pl.when(i == 0)
        def _():
            carry[...] = jnp.zeros_like(carry)

        ub_ref[...] = jnp.dot(h_ref[...], wb_ref[...], preferred_element_type=F32)
        bd = _head_ones()
        _, _, qn = _head_rms(ub_ref[:, 0:512], qg_ref[...], bd)
        _, _, kn = _head_rms(ub_ref[:, 512:1024], kg_ref[...], bd)
        q_ref[...] = (qn * ATT_SCALE).astype(BF16)
        k_ref[...] = kn.astype(BF16)
        v_ref[...] = ub_ref[:, 1024:1536].astype(BF16)
        lane = lax.broadcasted_iota(jnp.int32, (1, LANES), 1)
        logf = jnp.where(lane < N_HEADS, _log_sigmoid(ub_ref[:, 2048:2176] + fb_ref[...]), 0.0)
        cum = _exact_dot(logf, _tri(tm, True), ones_first=True) + carry[...]
        for h in range(N_HEADS):
            cc_ref[h] = jnp.broadcast_to(cum[:, h:h + 1], (tm, LANES))
        cr_ref[...] = jnp.transpose(cum)[0:N_HEADS, :]
        carry[...] = cum[tm - 1:tm, :]

    blk = pl.BlockSpec((tm, D_HALF), lambda i: (i, 0))
    return pl.pallas_call(
        body, name="fox_front", grid=(s // tm,),
        in_specs=[pl.BlockSpec((tm, d), lambda i: (i, 0)),
                  pl.BlockSpec(w_b.shape, lambda i: (0, 0), pipeline_mode=pl.Buffered(1)),
                  pl.BlockSpec((1, LANES), lambda i: (0, 0)),
                  pl.BlockSpec((1, D_HALF), lambda i: (0, 0)), pl.BlockSpec((1, D_HALF), lambda i: (0, 0))],
        out_specs=[pl.BlockSpec((tm, SEC), lambda i: (i, 0)), blk, blk, blk,
                   pl.BlockSpec((N_HEADS, tm, LANES), lambda i: (0, i, 0)), pl.BlockSpec((N_HEADS, tm), lambda i: (0, i))],
        out_shape=[jax.ShapeDtypeStruct((s, SEC), F32)] + [jax.ShapeDtypeStruct((s, D_HALF), BF16)] * 3
                  + [jax.ShapeDtypeStruct((N_HEADS, s, LANES), F32), jax.ShapeDtypeStruct((N_HEADS, s), F32)],
        scratch_shapes=[pltpu.VMEM((1, LANES), F32)],
        compiler_params=_params("arbitrary"),
    )(h, w_b, fb, qg, kg)


ATT_T = 256


def _tiles(nblk, by_query):
    if by_query:
        pairs = [(i, j) for i in range(nblk) for j in range(i + 1)]
    else:
        pairs = [(i, j) for j in range(nblk) for i in range(j, nblk)]
    return (jnp.asarray([p[0] for p in pairs], jnp.int32), jnp.asarray([p[1] for p in pairs], jnp.int32))


def _attn_fwd(q, k, v, cc, cr):
    s = q.shape[0]
    t = ATT_T
    nblk = s // t

    def body(qi_ref, kj_ref, q_ref, k_ref, v_ref, cc_ref, cr_ref, o_ref, lse_ref, m_sc, l_sc, acc_sc):
        i = qi_ref[pl.program_id(0)]
        j = kj_ref[pl.program_id(0)]

        @pl.when(j == 0)
        def _():
            m_sc[...] = jnp.full_like(m_sc, NEG)
            l_sc[...] = jnp.zeros_like(l_sc)
            acc_sc[...] = jnp.zeros_like(acc_sc)

        def tile(on_diagonal):
            causal = _causal_tile(t) if on_diagonal else None
            left = lax.broadcasted_iota(jnp.int32, (1, LANES), 1) < HEAD
            for p in range(N_PAIRS):
                lanes = slice(p * LANES, (p + 1) * LANES)
                q2, k2, v2 = q_ref[:, lanes], k_ref[:, lanes], v_ref[:, lanes]
                acc2 = acc_sc[:, lanes]
                for e in range(2):
                    h = 2 * p + e
                    msk = left if e == 0 else jnp.logical_not(left)
                    sc = _dot_nt(jnp.where(msk, q2, jnp.zeros_like(q2)), k2)
                    sc = sc + (_wide(cc_ref[h]) - cr_ref[h:h + 1, :])
                    if on_diagonal:
                        sc = jnp.where(causal, sc, NEG)
                    m_prev = m_sc[h]
                    m_new = jnp.maximum(m_prev, jnp.max(sc, axis=1, keepdims=True))
                    alpha = jnp.exp(m_prev - m_new)
                    pm = jnp.exp(sc - _wide(m_new))
                    l_sc[h] = alpha * l_sc[h] + jnp.sum(pm, axis=1, keepdims=True)
                    m_sc[h] = m_new
                    pv = jnp.dot(pm.astype(BF16), v2, preferred_element_type=F32)
                    acc2 = jnp.where(msk, alpha * acc2 + pv, acc2)
                acc_sc[:, lanes] = acc2

        pl.when(j < i)(functools.partial(tile, False))
        pl.when(j == i)(functools.partial(tile, True))

        @pl.when(j == i)
        def _():
            left = lax.broadcasted_iota(jnp.int32, (1, LANES), 1) < HEAD
            for p in range(N_PAIRS):
                lanes = slice(p * LANES, (p + 1) * LANES)
                inv = jnp.where(left, 1.0 / l_sc[2 * p], 1.0 / l_sc[2 * p + 1])
                o_ref[:, lanes] = acc_sc[:, lanes] * inv
            for h in range(N_HEADS):
                lse_ref[h] = m_sc[h] + jnp.log(l_sc[h])

    qi, kj = _tiles(nblk, by_query=True)
    qblk = pl.BlockSpec((t, D_HALF), lambda n, qi, kj: (qi[n], 0))
    kblk = pl.BlockSpec((t, D_HALF), lambda n, qi, kj: (kj[n], 0))
    qrep = pl.BlockSpec((N_HEADS, t, LANES), lambda n, qi, kj: (0, qi[n], 0))
    return pl.pallas_call(
        body, name="fox_attn_fwd",
        grid_spec=pltpu.PrefetchScalarGridSpec(
            num_scalar_prefetch=2, grid=(qi.shape[0],),
            in_specs=[qblk, kblk, kblk, qrep, pl.BlockSpec((N_HEADS, t), lambda n, qi, kj: (0, kj[n]))],
            out_specs=[qblk, qrep],
            scratch_shapes=[pltpu.VMEM((N_HEADS, t, LANES), F32), pltpu.VMEM((N_HEADS, t, LANES), F32),
                            pltpu.VMEM((t, D_HALF), F32)]),
        out_shape=[jax.ShapeDtypeStruct((s, D_HALF), F32), jax.ShapeDtypeStruct((N_HEADS, s, LANES), F32)],
        compiler_params=_params("arbitrary"),
    )(qi, kj, q, k, v, cc, cr)


def _causal_tile(t):
    return lax.broadcasted_iota(jnp.int32, (t, t), 0) >= lax.broadcasted_iota(jnp.int32, (t, t), 1)


def _wide(x):
    return jnp.concatenate([x, x], axis=1)


def _attn_probs(q2, k2, v2, do2, msk, causal, bias, lse_rows):
    zero = jnp.zeros_like(q2)
    qh = jnp.where(msk, q2, zero)
    doh = jnp.where(msk, do2, zero)
    sc = _dot_nt(qh, k2) + bias
    if causal is not None:
        sc = jnp.where(causal, sc, NEG)
    pm = jnp.exp(sc - _wide(lse_rows))
    dp = _dot_nt(doh, v2)
    return qh, doh, pm, dp


def _attn_bwd_rowdot(q, k, v, do, lse, cc, cr):
    s = q.shape[0]
    t = ATT_T
    nblk = s // t

    def body(qi_ref, kj_ref, q_ref, k_ref, v_ref, do_ref, lse_ref, cc_ref, cr_ref, dd_ref, acc):
        i = qi_ref[pl.program_id(0)]
        j = kj_ref[pl.program_id(0)]

        @pl.when(j == 0)
        def _():
            acc[...] = jnp.zeros_like(acc)

        def tile(on_diagonal):
            causal = _causal_tile(t) if on_diagonal else None
            left = lax.broadcasted_iota(jnp.int32, (1, LANES), 1) < HEAD
            for p in range(N_PAIRS):
                lanes = slice(p * LANES, (p + 1) * LANES)
                q2, k2, v2, do2 = q_ref[:, lanes], k_ref[:, lanes], v_ref[:, lanes], do_ref[:, lanes]
                for e in range(2):
                    h = 2 * p + e
                    msk = left if e == 0 else jnp.logical_not(left)
                    bias = _wide(cc_ref[h]) - cr_ref[h:h + 1, :]
                    _, _, pm, dp = _attn_probs(q2, k2, v2, do2, msk, causal, bias, lse_ref[h])
                    acc[h] += jnp.sum(pm * dp, axis=1, keepdims=True)

        pl.when(j < i)(functools.partial(tile, False))
        pl.when(j == i)(functools.partial(tile, True))

        @pl.when(j == i)
        def _():
            dd_ref[...] = acc[...]

    qi, kj = _tiles(nblk, by_query=True)
    qblk = pl.BlockSpec((t, D_HALF), lambda n, qi, kj: (qi[n], 0))
    qcol = pl.BlockSpec((N_HEADS, t, LANES), lambda n, qi, kj: (0, qi[n], 0))
    kblk = pl.BlockSpec((t, D_HALF), lambda n, qi, kj: (kj[n], 0))
    return pl.pallas_call(
        body, name="fox_attn_rowdot",
        grid_spec=pltpu.PrefetchScalarGridSpec(
            num_scalar_prefetch=2, grid=(qi.shape[0],),
            in_specs=[qblk, kblk, kblk, qblk, qcol, qcol, pl.BlockSpec((N_HEADS, t), lambda n, qi, kj: (0, kj[n]))],
            out_specs=qcol, scratch_shapes=[pltpu.VMEM((N_HEADS, t, LANES), F32)]),
        out_shape=jax.ShapeDtypeStruct((N_HEADS, s, LANES), F32),
        compiler_params=_params("arbitrary"),
    )(qi, kj, q, k, v, do, lse, cc, cr)


def _attn_bwd(q, k, v, do, lse, dd, cc, cr):
    s = q.shape[0]
    t = ATT_T
    nblk = s // t

    def body(qi_ref, kj_ref, q_ref, k_ref, v_ref, do_ref, lse_ref, dd_ref, cc_ref, cr_ref,
             dq_ref, dk_ref, dv_ref, dcr_ref, dk_sc, dv_sc, dcr_sc):
        i = qi_ref[pl.program_id(0)]
        j = kj_ref[pl.program_id(0)]

        @pl.when(pl.program_id(0) == 0)
        def _():
            dq_ref[...] = jnp.zeros_like(dq_ref)

        @pl.when(i == j)
        def _():
            dk_sc[...] = jnp.zeros_like(dk_sc)
            dv_sc[...] = jnp.zeros_like(dv_sc)
            dcr_sc[...] = jnp.zeros_like(dcr_sc)

        def tile(on_diagonal):
            causal = _causal_tile(t) if on_diagonal else None
            left = lax.broadcasted_iota(jnp.int32, (1, LANES), 1) < HEAD
            qrows = pl.ds(pl.multiple_of(i * t, t), t)
            for p in range(N_PAIRS):
                lanes = slice(p * LANES, (p + 1) * LANES)
                q2, k2, v2, do2 = q_ref[:, lanes], k_ref[:, lanes], v_ref[:, lanes], do_ref[:, lanes]
                zero = jnp.zeros_like(q2)
                dq2 = jnp.zeros((t, LANES), F32)
                dk2 = jnp.zeros((t, LANES), F32)
                dv2 = jnp.zeros((t, LANES), F32)
                for e in range(2):
                    h = 2 * p + e
                    msk = left if e == 0 else jnp.logical_not(left)
                    bias = _wide(cc_ref[h]) - cr_ref[h:h + 1, :]
                    qh, doh, pm, dp = _attn_probs(q2, k2, v2, do2, msk, causal, bias, lse_ref[h])
                    dsc = pm * (dp - _wide(dd_ref[h]))
                    dsb = dsc.astype(BF16)
                    dv2 += _dot_tn(pm.astype(BF16), doh)
                    dk2 += _dot_tn(dsb, qh)
                    dq2 += jnp.dot(dsb, jnp.where(msk, k2, zero), preferred_element_type=F32)
                    dcr_sc[h:h + 1, :] += -_colsum(dsc)
                dq_ref[qrows, lanes] += dq2 * ATT_SCALE
                dk_sc[:, lanes] += dk2
                dv_sc[:, lanes] += dv2

        pl.when(i > j)(functools.partial(tile, False))
        pl.when(i == j)(functools.partial(tile, True))

        @pl.when(i == nblk - 1)
        def _():
            dk_ref[...] = dk_sc[...]
            dv_ref[...] = dv_sc[...]
            dcr_ref[...] = dcr_sc[...]

    qi, kj = _tiles(nblk, by_query=False)
    qblk = pl.BlockSpec((t, D_HALF), lambda n, qi, kj: (qi[n], 0))
    qcol = pl.BlockSpec((N_HEADS, t, LANES), lambda n, qi, kj: (0, qi[n], 0))
    kblk = pl.BlockSpec((t, D_HALF), lambda n, qi, kj: (kj[n], 0))
    krow = pl.BlockSpec((N_HEADS, t), lambda n, qi, kj: (0, kj[n]))
    return pl.pallas_call(
        body, name="fox_attn_bwd",
        grid_spec=pltpu.PrefetchScalarGridSpec(
            num_scalar_prefetch=2, grid=(qi.shape[0],),
            in_specs=[qblk, kblk, kblk, qblk, qcol, qcol, qcol, krow],
            out_specs=[pl.BlockSpec((s, D_HALF), lambda n, qi, kj: (0, 0)), kblk, kblk, krow],
            scratch_shapes=[pltpu.VMEM((t, D_HALF), F32), pltpu.VMEM((t, D_HALF), F32), pltpu.VMEM((N_HEADS, t), F32)]),
        out_shape=[jax.ShapeDtypeStruct((s, D_HALF), F32)] * 3 + [jax.ShapeDtypeStruct((N_HEADS, s), F32)],
        compiler_params=_params("arbitrary"),
    )(qi, kj, q, k, v, do, lse, dd, cc, cr)


def _fox_prep_bwd(u_b, h_t, dq, dk, dv, dgate, dcum, fb, qg, kg, tm=256):
    s = u_b.shape[0]
    nb = s // tm
    d = h_t.shape[0]

    def body(ub_ref, ht_ref, dq_ref, dk_ref, dv_ref, dg_ref, dc_ref, fb_ref, qg_ref, kg_ref,
             du_ref, dwb_ref, dqg_ref, dkg_ref, dfb_ref, carry):
        i = pl.program_id(0)

        @pl.when(i == 0)
        def _():
            carry[...] = jnp.zeros_like(carry)
            dwb_ref[...] = jnp.zeros_like(dwb_ref)
            dqg_ref[...] = jnp.zeros_like(dqg_ref)
            dkg_ref[...] = jnp.zeros_like(dkg_ref)
            dfb_ref[...] = jnp.zeros_like(dfb_ref)

        bd = _head_ones()
        for lo, g_ref, d_ref, dgain_ref in ((0, qg_ref, dq_ref, dqg_ref), (512, kg_ref, dk_ref, dkg_ref)):
            gain = g_ref[...]
            xh, rinv, _ = _head_rms(ub_ref[:, lo:lo + 512], gain, bd)
            dn = d_ref[...]
            dgain_ref[...] += _colsum(dn * xh)
            dxh = dn * gain
            du_ref[:, lo:lo + 512] = rinv * (dxh - xh * (_head_sum(dxh * xh, bd) * (1.0 / HEAD)))
        du_ref[:, 1024:1536] = dv_ref[...]
        du_ref[:, 1536:2048] = dg_ref[...]
        lane = lax.broadcasted_iota(jnp.int32, (1, LANES), 1)
        dc = dc_ref[...]
        dlogf = _exact_dot(dc, _tri(tm, False), ones_first=True) + carry[...]
        carry[...] += _colsum(dc)
        fl = ub_ref[:, 2048:2176] + fb_ref[...]
        dfl = jnp.where(lane < N_HEADS, dlogf * (1.0 - _sigmoid(fl)), 0.0)
        du_ref[:, 2048:2176] = dfl
        dfb_ref[...] += _colsum(dfl)
        dwb_ref[...] += jnp.dot(ht_ref[...], du_ref[...].astype(BF16), preferred_element_type=F32)

    rev = lambda w: pl.BlockSpec((tm, w), lambda i: (nb - 1 - i, 0))
    vec = lambda w: pl.BlockSpec((1, w), lambda i: (0, 0))
    return pl.pallas_call(
        body, name="fox_prep_bwd", grid=(nb,),
        in_specs=[rev(SEC), pl.BlockSpec((d, tm), lambda i: (0, nb - 1 - i))] + [rev(D_HALF)] * 4
                 + [rev(LANES), vec(LANES), vec(D_HALF), vec(D_HALF)],
        out_specs=[rev(SEC), pl.BlockSpec((d, SEC), lambda i: (0, 0)), vec(D_HALF), vec(D_HALF), vec(LANES)],
        out_shape=[jax.ShapeDtypeStruct((s, SEC), F32), jax.ShapeDtypeStruct((d, SEC), F32),
                   jax.ShapeDtypeStruct((1, D_HALF), F32), jax.ShapeDtypeStruct((1, D_HALF), F32),
                   jax.ShapeDtypeStruct((1, LANES), F32)],
        scratch_shapes=[pltpu.VMEM((1, LANES), F32)],
        compiler_params=_params("arbitrary"),
    )(u_b, h_t, dq, dk, dv, dgate, dcum, fb, qg, kg)


def _merge(y, r, k, v, gate_a, o, u_b, h, x, tgt, w_g, wa, wb, wo, fg, lw, lb, rk, tm=256):
    s, d = x.shape

    def body(y_ref, r_ref, k_ref, v_ref, ga_ref, o_ref, gb_ref, h_ref, x_ref, t_ref, wg_ref, wa_ref, wb_ref, wo_ref,
             fg_ref, lw_ref, lb_ref, rk_ref,
             dx2_ref, dy_ref, drb_ref, dkb_ref, dvb_ref, dga_ref, do_ref, dgb_ref, dug_ref,
             dwa_ref, dwb_ref, dwo_ref, dfg_ref, loss_ref, dlw_ref, dlb_ref, drk_ref):
        i = pl.program_id(0)

        @pl.when(i == 0)
        def _():
            for ref in (dwa_ref, dwb_ref, dwo_ref, dfg_ref, loss_ref, dlw_ref, dlb_ref, drk_ref):
                ref[...] = jnp.zeros_like(ref)

        bd = _head_ones()
        wa_v, wb_v, wo_v, fg_v = wa_ref[...], wb_ref[...], wo_ref[...], fg_ref[...]
        rv, kv, vv, ga, lw_v, rk_v = r_ref[...], k_ref[...], v_ref[...], ga_ref[...], lw_ref[...], rk_ref[...]
        yn, rstd, rkk, sga, pre = _rwkv_post_math(y_ref[...], rv, kv, vv, ga, lw_v, lb_ref[...], rk_v, bd)
        silu_a = ga * sga
        gb, ov = gb_ref[...], o_ref[...]
        sgb = _sigmoid(gb)
        silu_b = gb * sgb
        ma = (pre * silu_a).astype(BF16)
        mb = (ov * silu_b).astype(BF16)
        ya = jnp.dot(ma, wa_v, preferred_element_type=F32)
        yb = jnp.dot(mb, wb_v, preferred_element_type=F32)
        ug = jnp.dot(h_ref[...], wg_ref[...], preferred_element_type=F32)
        sa = _sigmoid(ug[:, 0:d])
        sb = _sigmoid(ug[:, d:2 * d])
        merged = (sa * ya + sb * yb).astype(BF16)
        x2 = x_ref[...] + jnp.dot(merged, wo_v, preferred_element_type=F32)
        r2 = lax.rsqrt(jnp.mean(x2 * x2, axis=-1, keepdims=True) + RMS_EPS)
        x2h = x2 * r2
        err = x2h * fg_v - t_ref[...]
        loss_ref[...] += _colsum(err * err)
        dyo = err * (1.0 / d)
        dfg_ref[...] += _colsum(dyo * x2h)
        dx2h = dyo * fg_v
        dx2 = r2 * (dx2h - x2h * jnp.mean(dx2h * x2h, axis=-1, keepdims=True))
        dx2_ref[...] = dx2
        dx2b = dx2.astype(BF16)
        dmerged = _dot_nt(dx2b, wo_v)
        dwo_ref[...] += _dot_tn(merged, dx2b)
        dya = dmerged * sa
        dyb = dmerged * sb
        dug_ref[:, 0:d] = dya * ya * (1.0 - sa)
        dug_ref[:, d:2 * d] = dyb * yb * (1.0 - sb)
        dyab = dya.astype(BF16)
        dybb = dyb.astype(BF16)
        dwa_ref[...] += _dot_tn(ma, dyab)
        dwb_ref[...] += _dot_tn(mb, dybb)
        dmb = _dot_nt(dybb, wb_v)
        do_ref[...] = (dmb * silu_b).astype(BF16)
        dgb_ref[...] = dmb * ov * (sgb * (1.0 + gb * (1.0 - sgb)))
        dma = _dot_nt(dyab, wa_v)
        dga_ref[...] = dma * pre * (sga * (1.0 + ga * (1.0 - sga)))
        dpre = dma * silu_a
        dlw_ref[...] += _colsum(dpre * yn)
        dlb_ref[...] += _colsum(dpre)
        dyn = dpre * lw_v
        m1 = _head_sum(dyn, bd) * (1.0 / HEAD)
        m2 = _head_sum(dyn * yn, bd) * (1.0 / HEAD)
        dy_ref[...] = rstd * (dyn - m1 - yn * m2)
        dvb_ref[...] = dpre * rkk
        drkk = _head_sum(dpre * vv, bd)
        drb_ref[...] = drkk * kv * rk_v
        dkb_ref[...] = drkk * rv * rk_v
        drk_ref[...] += _colsum(drkk * rv * kv)

    row = lambda w: pl.BlockSpec((tm, w), lambda i: (i, 0))
    full = lambda a: pl.BlockSpec(a.shape, lambda i: (0, 0))
    once = lambda a: pl.BlockSpec(a.shape, lambda i: (0, 0), pipeline_mode=pl.Buffered(1))
    half = jax.ShapeDtypeStruct((s, D_HALF), F32)
    fshape = lambda a: jax.ShapeDtypeStruct(a.shape, F32)
    return pl.pallas_call(
        body, name="merge_fwd_bwd", grid=(s // tm,),
        in_specs=[row(D_HALF)] * 6 + [pl.BlockSpec((tm, D_HALF), lambda i: (i, 3)), row(d), row(d), row(d),
                                      once(w_g), once(wa), once(wb), once(wo), full(fg), full(lw), full(lb), full(rk)],
        out_specs=[row(d)] + [row(D_HALF)] * 7 + [row(GATE_COLS), full(wa), full(wb), full(wo), full(fg), full(fg),
                                                   full(lw), full(lb), full(rk)],
        out_shape=[jax.ShapeDtypeStruct((s, d), F32)] + [half] * 5 + [jax.ShapeDtypeStruct((s, D_HALF), BF16), half,
                                                                    jax.ShapeDtypeStruct((s, GATE_COLS), F32),
                                                                    fshape(wa), fshape(wb), fshape(wo), fshape(fg),
                                                                    fshape(fg), fshape(lw), fshape(lb), fshape(rk)],
        compiler_params=_params("arbitrary"),
    )(y, r, k, v, gate_a, o, u_b, h, x, tgt, w_g, wa, wb, wo, fg, lw, lb, rk)


def _lora_weight(w_up, a_up):
    z = jnp.zeros((LORA, D_HALF), w_up.dtype)
    return jnp.concatenate([jnp.concatenate([w_up, z], axis=1), jnp.concatenate([z, a_up], axis=1)], axis=0)


def _device_grads(x, tgt, p, w_a, w_up, a_up, late_weights, fwd_exchange=None, bwd_exchange=None, tail_exchange=None):
    wl = _lora_weight(w_up, a_up)
    rk = p["r_k"].reshape(1, D_HALF)
    fb = jnp.pad(p["f_bias"], ((0, 0), (0, LANES - N_HEADS)))
    qg = jnp.tile(p["q_norm_g"], (1, N_HEADS))
    kg = jnp.tile(p["k_norm_g"], (1, N_HEADS))
    fg = p["final_norm_g"].reshape(1, D_MODEL)
    mixer = (p["shift_mu"], wl, p["w0"], p["a0"], p["k_k"], p["k_a"])

    h, u_a, r, dec, k, v, av, bv, gate_a = _rwkv_front(x, p["norm_g"], w_a, *mixer)
    y, st, arrived = _wkv_fwd(r, dec, k, av, bv, v, fwd_exchange)

    w_b, w_g, w_out_a, w_out_b, w_out = late_weights(arrived)
    u_b, q, kn, vb, cc, cr = _fox_front(h, w_b, fb, qg, kg)
    o, lse = _attn_fwd(q, kn, vb, cc, cr)

    (dx2, dy, dr_b, dk_b, dv_b, dgate_a, do, dgate_b, du_g, dwa, dwb, dwo, dfg, loss_vec, dlw, dlb, drk) = _merge(
        y, r, k, v, gate_a, o, u_b, h, x, tgt, w_g, w_out_a, w_out_b, w_out, fg, p["lnx_w"], p["lnx_b"], rk)

    dd = _attn_bwd_rowdot(q, kn, vb, do, lse, cc, cr)
    dq, dk_att, dv_att, dcr = _attn_bwd(q, kn, vb, do, lse, dd, cc, cr)
    dcum = jnp.pad(dcr.T, ((0, 0), (0, LANES - N_HEADS)))
    h_t = h.T
    du_b, dw_b, dqg, dkg, dfb = _fox_prep_bwd(u_b, h_t, dq, dk_att, dv_att, dgate_b, dcum, fb, qg, kg)
    dw_g = _matmul_tn_acc(h_t, du_g, "dw_gate")

    scan_grads, sent = _wkv_bwd(r, dec, k, av, bv, v, dy, st,
                                bwd_exchange(dw_b, dw_g, dwa, dwb, dwo) if bwd_exchange else None)
    du_a, dw_a, dmu, dwl, dw0, da0, dkkw, dkaw = _rwkv_prep_bwd(
        u_a, h_t, (*scan_grads, dr_b, dk_b, dv_b, dgate_a), *mixer)
    dw_up, da_up = dwl[:LORA, :D_HALF], dwl[LORA:, D_HALF:]
    sent_last = _run_on_sequencer(tail_exchange(dw_a, dw_up, da_up), "scatter_tail", 1) if tail_exchange else []
    grad_x, dnorm_g, _ = _inproj_bwd(du_a, du_b, du_g, w_a, w_b, w_g, x, dx2, p["norm_g"])

    grads = dict(
        norm_g=dnorm_g, w_in=(dw_a, dw_b, dw_g), shift_mu=dmu,
        w_lora_up=dw_up, w0=dw0, a_lora_up=da_up, a0=da0, k_k=dkkw, k_a=dkaw,
        r_k=drk.reshape(1, N_HEADS, HEAD), lnx_w=dlw, lnx_b=dlb, f_bias=dfb[:, :N_HEADS],
        q_norm_g=dqg.reshape(N_HEADS, HEAD).sum(axis=0, keepdims=True),
        k_norm_g=dkg.reshape(N_HEADS, HEAD).sum(axis=0, keepdims=True),
        w_out_a=dwa, w_out_b=dwb, w_out=dwo, final_norm_g=dfg.reshape(D_MODEL))
    return loss_vec, grad_x, grads, sent, sent_last


CHIP_FLIPS = ((1, 0), (0, 1), (1, 1))
ANY = pl.BlockSpec(memory_space=pl.ANY)


def _position():
    return lax.axis_index("x"), lax.axis_index("y"), lax.axis_index("c")


def _flip(v, f):
    return 1 - v if f else v


def _both(a, b):
    if a is None:
        return b
    return a if b is None else jnp.logical_and(a, b)


def _when(cond, fn):
    if cond is None:
        fn()
    else:
        pl.when(cond)(fn)


class _Moves:
    def __init__(self, send_sems, recv_sems, local_sems):
        self.send_sems, self.recv_sems, self.local_sems = send_sems, recv_sems, local_sems
        self.remote, self.local = [], []

    def send(self, src, dst, peer, landing, send_if=None, recv_if=None, first=False):
        k = len(self.remote)
        sems = dict(send_sem=self.send_sems.at[k], recv_sem=self.recv_sems.at[k], device_id=peer, device_id_type=MESH)
        out = pltpu.make_async_remote_copy(src_ref=src, dst_ref=dst, **sems)
        arrival = pltpu.make_async_remote_copy(src_ref=src, dst_ref=landing, **sems)
        self.remote.append((out, arrival, send_if, recv_if, first))

    def copy(self, src, dst, cond=None):
        cp = pltpu.make_async_copy(src, dst, self.local_sems.at[len(self.local)])
        self.local.append((cp, cond))

    def start(self, also=None):
        for cp, cond in self.local:
            _when(_both(also, cond), cp.start)
        for out, _, send_if, _, _ in self.remote:
            _when(_both(also, send_if), out.start)

    def wait_arrivals(self, also=None, first=None):
        for _, arrival, _, recv_if, is_first in self.remote:
            if first is None or first == is_first:
                _when(_both(also, recv_if), arrival.wait_recv)

    def wait_sent(self, also=None):
        for out, _, send_if, _, _ in self.remote:
            _when(_both(also, send_if), out.wait_send)
        for cp, cond in self.local:
            _when(_both(also, cond), cp.wait)

    def wait(self, also=None):
        self.wait_arrivals(also)
        self.wait_sent(also)


class _Exchange:
    def __init__(self, operands, out_shapes, n_remote, n_local, build, relays=None, in_place=(), n_staging=0):
        self.operands, self.out_shapes = list(operands), list(out_shapes)
        self.n_remote, self.n_local, self.build = n_remote, n_local, build
        self.relays, self.in_place = relays, in_place
        self.n_staging = n_staging

    def scratch(self):
        return [pltpu.SemaphoreType.DMA((self.n_remote,)), pltpu.SemaphoreType.DMA((self.n_remote,)),
                pltpu.SemaphoreType.DMA((max(self.n_local, 1),))]

    def moves(self, in_refs, out_refs, sems):
        mv = _Moves(*sems)
        self.build(mv, in_refs, out_refs)
        return mv


def _run_on_sequencer(exchange, name, collective_id):
    ins = [jax.new_ref(a, memory_space=pltpu.MemorySpace.HBM) for a in exchange.operands]
    outs = [ins[i] if i in exchange.in_place else jax.empty_ref(s, memory_space=pltpu.MemorySpace.HBM)
            for i, s in enumerate(exchange.out_shapes)]
    forward, to_sibling = exchange.relays or (None, None)
    relay_scratch = [pltpu.SemaphoreType.DMA((stage[0],)) for stage in (forward, to_sibling) if stage for _ in range(2)]

    def launch(*sems):
        x, y, c = _position()
        peers = [(_flip(x, fx), _flip(y, fy), c) for fx, fy in CHIP_FLIPS] + ([(x, y, 1 - c)] if to_sibling else [])
        barrier = pltpu.get_barrier_semaphore()
        for peer in peers:
            pl.semaphore_signal(barrier, inc=1, device_id=peer, device_id_type=MESH)
        pl.semaphore_wait(barrier, len(peers))
        moves = exchange.moves(ins, outs, sems[:3])
        moves.start()
        later = []
        if forward:
            onward = _Moves(sems[3], sems[4], None)
            forward[1](onward, ins, outs)
            moves.wait_arrivals(first=True)
            onward.start()
            moves.wait_arrivals(first=False)
            onward.wait_arrivals()
            later.append(onward)
        else:
            moves.wait_arrivals()
        if to_sibling:
            passed = _Moves(*sems[-2:], None)
            to_sibling[1](passed, ins, outs)
            passed.start()
            passed.wait_arrivals()
            later.append(passed)
        for mv in later + [moves]:
            mv.wait_sent()

    pl.kernel(launch, mesh=plsc.ScalarSubcoreMesh(axis_name="sequencer", num_cores=1), name=name,
              scratch_types=tuple(exchange.scratch() + relay_scratch),
              compiler_params=pltpu.CompilerParams(collective_id=collective_id))()
    return [o[...] for o in outs[:len(outs) - exchange.n_staging]]


def _row_major_copy(a, name):
    r, c = a.shape
    tr = _row_tile(r)

    def body(a_ref, o_ref):
        o_ref[...] = a_ref[...]

    blk = pl.BlockSpec((tr, c), lambda i: (i, 0))
    return pl.pallas_call(body, name=name, grid=(r // tr,), in_specs=[blk], out_specs=blk,
                          out_shape=jax.ShapeDtypeStruct(a.shape, a.dtype), compiler_params=_params("parallel"))(a)


def _is_chip(x, y, chip):
    return jnp.logical_and(x == chip // 2, y == chip % 2)


def _gather_exchange(from_chip, from_all, split=()):
    n1, n2 = len(from_chip), len(from_all)
    near = CHIP_FLIPS[:2]

    def quarters(t, c, first, count=1):
        n = from_chip[t][1].shape[0] // 4
        return pl.ds((2 * c + first) * n, count * n)

    def build(mv, ins, outs):
        x, y, c = _position()
        me = 2 * x + y
        for t, (chip, _) in enumerate(from_chip):
            if t not in split:
                mv.copy(ins[t], outs[t], cond=_is_chip(x, y, chip))
        for t in range(n2):
            mv.copy(ins[n1 + t], outs[n1 + t].at[me])
        for t in split:
            for first in (True, False):
                for f, (fx, fy) in enumerate(near):
                    px, py = _flip(x, fx), _flip(y, fy)
                    part = quarters(t, c, f if first else 1 - f)
                    mv.send(ins[t].at[part], outs[t].at[part], (px, py, c), landing=outs[t].at[part], first=first,
                            send_if=_is_chip(x, y, from_chip[t][0]), recv_if=_is_chip(px, py, from_chip[t][0]))
        for fx, fy in CHIP_FLIPS:
            px, py = _flip(x, fx), _flip(y, fy)
            peer = (px, py, c)
            for t, (chip, _) in enumerate(from_chip):
                if t not in split:
                    mv.send(ins[t], outs[t], peer, landing=outs[t],
                            send_if=_is_chip(x, y, chip), recv_if=_is_chip(px, py, chip))
            for t in range(n2):
                mv.send(ins[n1 + t], outs[n1 + t].at[me], peer, landing=outs[n1 + t].at[2 * px + py])

    def forward(mv, ins, outs):
        x, y, c = _position()
        for t in split:
            chip = from_chip[t][0]
            for f, (fx, fy) in enumerate(near):
                gx, gy = near[1 - f]
                part = quarters(t, c, f)
                mv.send(outs[t].at[part], outs[t].at[part], (_flip(x, gx), _flip(y, gy), c), landing=outs[t].at[part],
                        send_if=_is_chip(_flip(x, fx), _flip(y, fy), chip), recv_if=_is_chip(1 - x, 1 - y, chip))

    def to_sibling(mv, ins, outs):
        x, y, c = _position()
        for t in split:
            came = jnp.logical_not(_is_chip(x, y, from_chip[t][0]))
            mv.send(outs[t].at[quarters(t, c, 0, 2)], outs[t].at[quarters(t, c, 0, 2)], (x, y, 1 - c),
                    landing=outs[t].at[quarters(t, 1 - c, 0, 2)], send_if=came, recv_if=came)

    arrays = [a for _, a in from_chip] + list(from_all)
    shapes = [jax.ShapeDtypeStruct(a.shape, a.dtype) for _, a in from_chip]
    shapes += [jax.ShapeDtypeStruct((N_CHIPS,) + a.shape, a.dtype) for a in from_all]
    n_remote = len(CHIP_FLIPS) * (n1 - len(split) + n2) + 2 * len(near) * len(split)
    relays = ((len(near) * len(split), forward), (len(split), to_sibling)) if split else None
    return _Exchange(arrays, shapes, n_remote, n1 + n2, build, relays, in_place=split)


def _scatter_exchange(to_chip, to_all, via_neighbours=False):
    n1, n2 = len(to_chip), len(to_all)
    near = CHIP_FLIPS[:2]
    direct = near if via_neighbours else CHIP_FLIPS

    def half(t, g):
        n = to_chip[t][1].shape[0] // 2
        return pl.ds(g * n, n)

    def build(mv, ins, outs):
        x, y, c = _position()
        if via_neighbours:
            for t, (chip, _) in enumerate(to_chip):
                for g, (gx, gy) in enumerate(near):
                    ox, oy = near[1 - g]
                    mv.send(ins[t].at[half(t, g)], outs[n1 + n2 + t], (_flip(x, gx), _flip(y, gy), c),
                            landing=outs[n1 + n2 + t], first=True, send_if=_is_chip(1 - x, 1 - y, chip),
                            recv_if=_is_chip(_flip(x, ox), _flip(y, oy), chip))
        for f, (fx, fy) in enumerate(CHIP_FLIPS):
            px, py = _flip(x, fx), _flip(y, fy)
            peer = (px, py, c)
            if (fx, fy) in direct:
                for t, (chip, _) in enumerate(to_chip):
                    mv.send(ins[t], outs[t].at[f], peer, landing=outs[t].at[f],
                            send_if=_is_chip(px, py, chip), recv_if=_is_chip(x, y, chip))
            for t in range(n2):
                mv.send(ins[n1 + t].at[2 * px + py], outs[n1 + t].at[f], peer, landing=outs[n1 + t].at[f])

    def forward(mv, ins, outs):
        x, y, c = _position()
        for t, (chip, _) in enumerate(to_chip):
            for g in range(len(near)):
                ox, oy = near[1 - g]
                far_slot = outs[t].at[len(near)].at[half(t, g)]
                mv.send(outs[n1 + n2 + t], far_slot, (_flip(x, ox), _flip(y, oy), c), landing=far_slot,
                        send_if=_is_chip(_flip(x, ox), _flip(y, oy), chip), recv_if=_is_chip(x, y, chip))

    arrays = [a for _, a in to_chip] + list(to_all)
    shapes = [jax.ShapeDtypeStruct((len(CHIP_FLIPS),) + a.shape, a.dtype) for _, a in to_chip]
    shapes += [jax.ShapeDtypeStruct((len(CHIP_FLIPS),) + a.shape[1:], a.dtype) for a in to_all]
    if not via_neighbours:
        return _Exchange(arrays, shapes, len(CHIP_FLIPS) * (n1 + n2), 0, build)
    shapes += [jax.ShapeDtypeStruct((a.shape[0] // 2, a.shape[1]), a.dtype) for _, a in to_chip]
    return _Exchange(arrays, shapes, 2 * len(near) * n1 + len(CHIP_FLIPS) * n2, 0, build,
                     relays=((len(near) * n1, forward), None), n_staging=n1)


def _swap_sibling(tensors, name):
    n = len(tensors)

    def body(*refs):
        ins, outs = refs[:n], refs[n:2 * n]
        send_sems, recv_sems = refs[2 * n:]
        x, y, c = _position()
        copies = [pltpu.make_async_remote_copy(
            src_ref=ins[t], dst_ref=outs[t], send_sem=send_sems.at[t], recv_sem=recv_sems.at[t],
            device_id=(x, y, 1 - c), device_id_type=MESH) for t in range(n)]
        for cp in copies:
            cp.start()
        for cp in copies:
            cp.wait_recv()
        for cp in copies:
            cp.wait_send()

    return pl.pallas_call(
        body, name=name, in_specs=[ANY] * n, out_specs=[ANY] * n,
        out_shape=[jax.ShapeDtypeStruct(a.shape, a.dtype) for a in tensors],
        scratch_shapes=[pltpu.SemaphoreType.DMA((n,)), pltpu.SemaphoreType.DMA((n,))],
        compiler_params=pltpu.CompilerParams(has_side_effects=True),
    )(*tensors)


def _pair_halves(g):
    r, cols = g.shape
    half = r // 2

    def body(g_ref, o_ref, mine, theirs, send_sem, recv_sem, local_sem):
        x, y, c = _position()
        away = pltpu.make_async_remote_copy(
            src_ref=g_ref.at[pl.ds((1 - c) * half, half)], dst_ref=theirs, send_sem=send_sem, recv_sem=recv_sem,
            device_id=(x, y, 1 - c), device_id_type=MESH)
        kept = pltpu.make_async_copy(g_ref.at[pl.ds(c * half, half)], mine, local_sem)
        away.start()
        kept.start()
        kept.wait()
        away.wait_recv()
        o_ref[...] = (mine[...] + theirs[...]).astype(BF16)
        away.wait_send()

    return pl.pallas_call(
        body, name="pair_halves", in_specs=[ANY], out_specs=pl.BlockSpec(memory_space=pltpu.VMEM),
        out_shape=jax.ShapeDtypeStruct((half, cols), BF16),
        scratch_shapes=[pltpu.VMEM((half, cols), F32), pltpu.VMEM((half, cols), F32),
                        pltpu.SemaphoreType.DMA(()), pltpu.SemaphoreType.DMA(()), pltpu.SemaphoreType.DMA(())],
        compiler_params=pltpu.CompilerParams(has_side_effects=True, vmem_limit_bytes=VMEM_LIMIT),
    )(g)


def _allreduce_small(slab):
    stages = 3

    def body(x_ref, o_ref, buf, send_sems, recv_sems):
        x, y, c = _position()
        peers = ((1 - x, y, c), (x, 1 - y, c), (x, y, 1 - c))
        o_ref[...] = x_ref[...]
        for k, peer in enumerate(peers):
            cp = pltpu.make_async_remote_copy(src_ref=o_ref, dst_ref=buf.at[k], send_sem=send_sems.at[k],
                                              recv_sem=recv_sems.at[k], device_id=peer, device_id_type=MESH)
            cp.start()
            cp.wait()
            o_ref[...] = o_ref[...] + buf[k]

    return pl.pallas_call(
        body, name="allreduce_small",
        in_specs=[pl.BlockSpec(memory_space=pltpu.VMEM)], out_specs=pl.BlockSpec(memory_space=pltpu.VMEM),
        out_shape=jax.ShapeDtypeStruct(slab.shape, slab.dtype),
        scratch_shapes=[pltpu.VMEM((stages,) + slab.shape, slab.dtype),
                        pltpu.SemaphoreType.DMA((stages,)), pltpu.SemaphoreType.DMA((stages,))],
        compiler_params=pltpu.CompilerParams(has_side_effects=True),
    )(slab)


def _row_tile(r):
    return min(r, 256)


def _sum4(stack, recv, me):
    _, r, c = stack.shape
    tr = _row_tile(r)

    def body(me_ref, own_ref, recv_ref, o_ref):
        o_ref[...] = (((own_ref[...] + recv_ref[0].astype(F32)) + recv_ref[1].astype(F32))
                      + recv_ref[2].astype(F32))

    return pl.pallas_call(
        body, name="sum_partials",
        grid_spec=pltpu.PrefetchScalarGridSpec(
            num_scalar_prefetch=1, grid=(r // tr,),
            in_specs=[pl.BlockSpec((None, tr, c), lambda i, me_ref: (me_ref[0], i, 0)),
                      pl.BlockSpec((len(CHIP_FLIPS), tr, c), lambda i, me_ref: (0, i, 0))],
            out_specs=pl.BlockSpec((tr, c), lambda i, me_ref: (i, 0))),
        out_shape=jax.ShapeDtypeStruct((r, c), F32), compiler_params=_params("parallel"),
    )(me, stack, recv)


def _sum_block(own, recv):
    r, c = own.shape
    tr = _row_tile(r)

    def body(own_ref, recv_ref, o_ref):
        o_ref[...] = (((own_ref[...] + recv_ref[0].astype(F32)) + recv_ref[1].astype(F32))
                      + recv_ref[2].astype(F32))

    return pl.pallas_call(
        body, name="sum_block", grid=(r // tr,),
        in_specs=[pl.BlockSpec((tr, c), lambda i: (i, 0)), pl.BlockSpec((len(CHIP_FLIPS), tr, c), lambda i: (0, i, 0))],
        out_specs=pl.BlockSpec((tr, c), lambda i: (i, 0)),
        out_shape=jax.ShapeDtypeStruct((r, c), F32), compiler_params=_params("parallel"),
    )(own, recv)


def _sum_half(own, recv, core):
    r, c = own.shape
    tr = _row_tile(r // 2)
    per_half = r // 2 // tr

    def body(core_ref, own_ref, recv_ref, o_ref):
        mine = pl.program_id(0) // per_half == core_ref[0]

        @pl.when(mine)
        def _():
            o_ref[...] = (((own_ref[...] + recv_ref[0].astype(F32)) + recv_ref[1].astype(F32))
                          + recv_ref[2].astype(F32))

        @pl.when(jnp.logical_not(mine))
        def _():
            o_ref[...] = own_ref[...]

    return pl.pallas_call(
        body, name="sum_half",
        grid_spec=pltpu.PrefetchScalarGridSpec(
            num_scalar_prefetch=1, grid=(r // tr,),
            in_specs=[pl.BlockSpec((tr, c), lambda i, core: (i, 0)),
                      pl.BlockSpec((len(CHIP_FLIPS), tr, c), lambda i, core: (0, i % per_half, 0))],
            out_specs=pl.BlockSpec((tr, c), lambda i, core: (i, 0))),
        out_shape=jax.ShapeDtypeStruct((r, c), F32), compiler_params=_params("parallel"),
    )(core, own, recv)


def _adamw_math(w, g, m, v):
    m = ADAM_B1 * m + (1.0 - ADAM_B1) * g
    v = ADAM_B2 * v + (1.0 - ADAM_B2) * (g * g)
    m_hat = m / (1.0 - ADAM_B1 ** ADAM_STEP)
    v_hat = v / (1.0 - ADAM_B2 ** ADAM_STEP)
    delta = -ADAM_LR * (m_hat / (jnp.sqrt(v_hat) + ADAM_EPS) + ADAM_WD * w)
    return delta, m, v


def _adamw(w, m, v, g_parts, name):
    r, c = w.shape
    tr = _row_tile(r)
    n = len(g_parts)

    def body(*refs):
        w_ref, m_ref, v_ref = refs[:3]
        g_refs = refs[3:3 + n]
        g_out, d_out, m_out, v_out, zero_out = refs[3 + n:]
        g = g_refs[0][...]
        for ref in g_refs[1:]:
            g = g + ref[...]
        g_out[...] = g
        d_out[...], m_out[...], v_out[...] = _adamw_math(w_ref[...], g, m_ref[...], v_ref[...])
        zero_out[0] = 0

    blk = pl.BlockSpec((tr, c), lambda i: (i, 0))
    return pl.pallas_call(
        body, name=name, grid=(r // tr,), in_specs=[blk] * (3 + n),
        out_specs=[blk] * 4 + [pl.BlockSpec(memory_space=pltpu.SMEM)],
        out_shape=[jax.ShapeDtypeStruct((r, c), F32)] * 4 + [jax.ShapeDtypeStruct((1,), jnp.int32)],
        compiler_params=_params("arbitrary"),
    )(w, m, v, *g_parts)


def _adamw_small(total, w, m, v):
    sizes = [w[n].size for n in SMALL]
    flat = lambda d: [d[n].reshape(1, -1) for n in SMALL]
    k = len(SMALL)

    def body(*refs):
        total_ref, w_refs, m_refs, v_refs = refs[0], refs[1:1 + k], refs[1 + k:1 + 2 * k], refs[1 + 2 * k:1 + 3 * k]
        outs = refs[1 + 3 * k:]
        for i, size in enumerate(sizes):
            g = total_ref[i:i + 1, 0:size]
            outs[i][...] = g
            outs[k + i][...], outs[2 * k + i][...], outs[3 * k + i][...] = _adamw_math(
                w_refs[i][...], g, m_refs[i][...], v_refs[i][...])

    res = pl.pallas_call(
        body, name="adamw_small", out_shape=[jax.ShapeDtypeStruct((1, size), F32) for size in sizes] * 4,
        compiler_params=_params(),
    )(total, *flat(w), *flat(m), *flat(v))
    return [{n: res[j * k + i].reshape(w[n].shape) for i, n in enumerate(SMALL)} for j in range(4)]


SHARDED = ("w_in", "w_lora_up", "a_lora_up", "w_out_a", "w_out_b", "w_out")
ROW_SHARDED = ("w_out",)
SMALL = ("norm_g", "shift_mu", "w0", "a0", "k_k", "k_a", "r_k", "lnx_w", "lnx_b", "f_bias", "q_norm_g", "k_norm_g",
         "final_norm_g")
WEIGHTS = ("norm_g", "w_in", "shift_mu", "w_lora_up", "w0", "a_lora_up", "a0", "k_k", "k_a", "r_k", "lnx_w", "lnx_b",
           "f_bias", "q_norm_g", "k_norm_g", "w_out_a", "w_out_b", "w_out", "final_norm_g")
SLAB_ROWS = 16
SLAB_COLS = SEC


def _to_slab(named, extra=None):
    rows = [jnp.pad(named[n].reshape(1, -1), ((0, 0), (0, SLAB_COLS - named[n].size))) for n in SMALL]
    if extra is not None:
        rows.append(jnp.pad(extra.reshape(1, -1), ((0, 0), (0, SLAB_COLS - extra.size))))
    rows.append(jnp.zeros((SLAB_ROWS - len(rows), SLAB_COLS), F32))
    return jnp.concatenate(rows, axis=0)


def _by_chip(g, name):
    if name in ROW_SHARDED:
        return g.reshape(N_CHIPS, g.shape[0] // N_CHIPS, g.shape[1])
    r, c = g.shape
    return g.reshape(r, N_CHIPS, c // N_CHIPS).transpose(1, 0, 2)


def _from_chips(stack, name):
    if name in ROW_SHARDED:
        return stack.reshape(-1, stack.shape[2])
    _, r, c = stack.shape
    return stack.transpose(1, 0, 2).reshape(r, N_CHIPS * c)


def kernel(x, norm_g, w_in, shift_mu, w_lora_up, w0, a_lora_up, a0, k_k, k_a, r_k, lnx_w, lnx_b, f_bias, q_norm_g, k_norm_g, w_out_a, w_out_b, w_out, final_norm_g, loss_target, m_norm_g, m_w_in, m_shift_mu, m_w_lora_up, m_w0, m_a_lora_up, m_a0, m_k_k, m_k_a, m_r_k, m_lnx_w, m_lnx_b, m_f_bias, m_q_norm_g, m_k_norm_g, m_w_out_a, m_w_out_b, m_w_out, m_final_norm_g, v_norm_g, v_w_in, v_shift_mu, v_w_lora_up, v_w0, v_a_lora_up, v_a0, v_k_k, v_k_a, v_r_k, v_lnx_w, v_lnx_b, v_f_bias, v_q_norm_g, v_k_norm_g, v_w_out_a, v_w_out_b, v_w_out, v_final_norm_g):
    w = dict(norm_g=norm_g, w_in=w_in, shift_mu=shift_mu, w_lora_up=w_lora_up, w0=w0, a_lora_up=a_lora_up, a0=a0,
             k_k=k_k, k_a=k_a, r_k=r_k, lnx_w=lnx_w, lnx_b=lnx_b, f_bias=f_bias, q_norm_g=q_norm_g,
             k_norm_g=k_norm_g, w_out_a=w_out_a, w_out_b=w_out_b, w_out=w_out, final_norm_g=final_norm_g)
    m = dict(norm_g=m_norm_g, w_in=m_w_in, shift_mu=m_shift_mu, w_lora_up=m_w_lora_up, w0=m_w0,
             a_lora_up=m_a_lora_up, a0=m_a0, k_k=m_k_k, k_a=m_k_a, r_k=m_r_k, lnx_w=m_lnx_w, lnx_b=m_lnx_b,
             f_bias=m_f_bias, q_norm_g=m_q_norm_g, k_norm_g=m_k_norm_g, w_out_a=m_w_out_a, w_out_b=m_w_out_b,
             w_out=m_w_out, final_norm_g=m_final_norm_g)
    v = dict(norm_g=v_norm_g, w_in=v_w_in, shift_mu=v_shift_mu, w_lora_up=v_w_lora_up, w0=v_w0,
             a_lora_up=v_a_lora_up, a0=v_a0, k_k=v_k_k, k_a=v_k_a, r_k=v_r_k, lnx_w=v_lnx_w, lnx_b=v_lnx_b,
             f_bias=v_f_bias, q_norm_g=v_q_norm_g, k_norm_g=v_k_norm_g, w_out_a=v_w_out_a, w_out_b=v_w_out_b,
             w_out=v_w_out, final_norm_g=v_final_norm_g)
    shapes = {n: w[n].shape for n in WEIGHTS}

    shard = {n: w[n][0].astype(BF16) for n in SHARDED}
    late = ("w_out_a", "w_out_b", "w_out")
    loras = ("w_lora_up", "a_lora_up")
    w_in_head, w_in_tail = shard["w_in"][:, :A_TAIL], shard["w_in"][:, A_TAIL:]
    shard0, shard1_head, up_stack, aup_stack = _run_on_sequencer(_gather_exchange(
        [(0, shard["w_in"]), (1, w_in_head)], [shard[n] for n in loras], split=(0, 1)), "gather_early", 2)
    moments = (_row_major_copy(m["w_in"][0], "m_w_in_rows"), _row_major_copy(v["w_in"][0], "v_w_in_rows"))
    shard0, moments = lax.optimization_barrier((shard0, moments))
    w_a = jnp.concatenate([shard0, shard1_head], axis=1)

    def late_weights(arrived):
        shard1_tail, shard2, shard3 = arrived[:3]
        w_b = jnp.concatenate([shard1_tail, shard2[:, :B_TAIL], jnp.zeros((D_MODEL, SEC - FOX_REAL), BF16)], axis=1)
        w_g = jnp.concatenate([shard2[:, B_TAIL:], shard3], axis=1)
        return (w_b, w_g, *[_from_chips(s, n) for n, s in zip(late, arrived[3:])])

    own = {}
    cut = {"block0": lambda: own["dw_a"][:, :SHARD_COLS], "head1": lambda: own["dw_a"][:, SHARD_COLS:],
           "tail1": lambda: own["dw_b"][:, :B_HEAD],
           "block2": lambda: jnp.concatenate([own["dw_b"][:, B_HEAD:FOX_REAL], own["dw_g"][:, :G_HEAD]], axis=1),
           "block3": lambda: own["dw_g"][:, G_HEAD:]}

    def bwd_exchange(dw_b, dw_g, dwa, dwb, dwo):
        own.update(dw_b=dw_b, dw_g=dw_g)
        own.update({n: _by_chip(g, n) for n, g in zip(late, (dwa, dwb, dwo))})
        return _scatter_exchange([(1, cut["tail1"]().astype(BF16)), (2, cut["block2"]().astype(BF16)),
                                  (3, cut["block3"]().astype(BF16))], [own[n].astype(BF16) for n in late])

    def tail_exchange(dw_a, dw_up, da_up):
        own.update(dw_a=dw_a)
        own.update({n: _by_chip(g, n) for n, g in zip(loras, (dw_up, da_up))})
        pair = _pair_halves(dw_a)
        return _scatter_exchange([(0, pair[:, :SHARD_COLS]), (1, pair[:, SHARD_COLS:])],
                                 [own[n].astype(BF16) for n in loras], via_neighbours=True)

    small = {n: w[n] for n in SMALL}
    loss_vec, grad_x, grads, sent, sent_last = _device_grads(
        x[0], loss_target[0], small, w_a, _from_chips(up_stack, "w_lora_up"), _from_chips(aup_stack, "a_lora_up"),
        late_weights, _gather_exchange([(1, w_in_tail), (2, shard["w_in"]), (3, shard["w_in"])], [shard[n] for n in late]),
        bwd_exchange, tail_exchange)

    total = _allreduce_small(_to_slab(grads, extra=loss_vec))
    loss = (0.5 / D_MODEL) * jnp.sum(total[len(SMALL)])
    out_g, out_d, out_m, out_v = _adamw_small(total, w, m, v)

    xpos, ypos, cpos = _position()
    me = (2 * xpos + ypos).astype(jnp.int32).reshape(1)
    core = cpos.astype(jnp.int32).reshape(1)
    core_sum, theirs = {}, {}

    def update(n):
        m_n, v_n = moments if n == "w_in" else (m[n][0], v[n][0])
        g, d, m2, v2, zero = _adamw(w[n][0], m_n, v_n, [core_sum[n], theirs[n]], "adamw_" + n)
        out_g[n], out_d[n], out_m[n], out_v[n] = (a.reshape(shapes[n]) for a in (g, d, m2, v2))
        return zero

    sent_last, out_d["norm_g"] = lax.optimization_barrier((sent_last, out_d["norm_g"]))
    core_sum["w_in"] = lax.switch(me[0], [
        lambda: _sum_half(cut["block0"](), sent_last[0], core),
        lambda: jnp.concatenate([_sum_half(cut["head1"](), sent_last[1], core), _sum_block(cut["tail1"](), sent[0])],
                                axis=1),
        lambda: _sum_block(cut["block2"](), sent[1]),
        lambda: _sum_block(cut["block3"](), sent[2])])
    core_sum.update({n: _sum4(own[n], r, me) for n, r in zip(loras, sent_last[2:])})
    rest = ("w_in",) + loras
    theirs.update(zip(rest, _swap_sibling([core_sum[n] for n in rest], "swap_sibling")))
    after_w_in = me + [update(n) for n in rest][0]
    core_sum.update({n: _sum4(own[n], r, after_w_in) for n, r in zip(late, sent[3:])})
    theirs.update(zip(late, _swap_sibling([core_sum[n] for n in late], "swap_sibling_late")))
    for n in late:
        update(n)

    return (loss, grad_x.reshape(x.shape), *[out_g[n] for n in WEIGHTS], *[out_d[n] for n in WEIGHTS],
            *[out_m[n] for n in WEIGHTS], *[out_v[n] for n in WEIGHTS])
```

```python
import functools
import math

import jax
import jax.numpy as jnp
from jax import lax
from jax.experimental import pallas as pl
from jax.experimental.pallas import tpu as pltpu
from jax.experimental.pallas import tpu_sc as plsc

F32 = jnp.float32
BF16 = jnp.bfloat16

D_MODEL = 1024
D_HALF = 512
HEAD = 64
N_HEADS = 8
LORA = 64
RWKV_COLS = 2176
FOX_REAL = 2056
SEC = 2176
GATE_COLS = 2048
IN_COLS = 6280
N_CHIPS = 4
SHARD_COLS = IN_COLS // N_CHIPS
A_TAIL = RWKV_COLS - SHARD_COLS
B_HEAD = SHARD_COLS - A_TAIL
B_TAIL = FOX_REAL - B_HEAD
G_HEAD = SHARD_COLS - B_TAIL
RMS_EPS = 1e-6
LNX_EPS = 64e-5
ATT_SCALE = HEAD ** -0.5
NEG = -1e30

ADAM_LR = 0.001
ADAM_B1 = 0.9
ADAM_B2 = 0.999
ADAM_EPS = 1e-08
ADAM_WD = 0.01
ADAM_STEP = 10

LANES = 128
SUBLANES = 8
VMEM_LIMIT = 56 * 1024 * 1024
MESH = pl.DeviceIdType.MESH


def _params(*sem):
    return pltpu.CompilerParams(dimension_semantics=sem if sem else None, vmem_limit_bytes=VMEM_LIMIT)


def _sigmoid(x):
    return 1.0 / (1.0 + jnp.exp(-x))


def _log_sigmoid(x):
    return jnp.minimum(x, 0.0) - jnp.log(1.0 + jnp.exp(-jnp.abs(x)))


def _head_ones():
    r = lax.broadcasted_iota(jnp.int32, (LANES, LANES), 0) >> 6
    c = lax.broadcasted_iota(jnp.int32, (LANES, LANES), 1) >> 6
    return (r == c).astype(BF16)


def _split3(x):
    hi = x.astype(BF16)
    r1 = x - hi.astype(F32)
    mid = r1.astype(BF16)
    lo = (r1 - mid.astype(F32)).astype(BF16)
    return hi, mid, lo


def _exact_dot(x, ones_bf16, ones_first=False):
    out = None
    for piece in _split3(x):
        if ones_first:
            t = jnp.dot(ones_bf16, piece, preferred_element_type=F32)
        else:
            t = jnp.dot(piece, ones_bf16, preferred_element_type=F32)
        out = t if out is None else out + t
    return out


def _head_sum(x, bd):
    n = x.shape[1] // LANES
    parts = [_exact_dot(x[:, i * LANES:(i + 1) * LANES], bd) for i in range(n)]
    return parts[0] if n == 1 else jnp.concatenate(parts, axis=1)


def _dot_nt(a, b):
    return lax.dot_general(a, b, (((1,), (1,)), ((), ())), preferred_element_type=F32)


def _dot_tn(a, b):
    return lax.dot_general(a, b, (((0,), (0,)), ((), ())), preferred_element_type=F32)


def _colsum(x):
    return jnp.sum(x, axis=0, keepdims=True)


def _matmul_tn_acc(at, b, name, tk=512):
    m, k = at.shape
    n = b.shape[1]

    def body(a_ref, b_ref, o_ref):
        j = pl.program_id(0)

        @pl.when(j == 0)
        def _():
            o_ref[...] = jnp.zeros_like(o_ref)

        o_ref[...] += jnp.dot(a_ref[...], b_ref[...].astype(BF16), preferred_element_type=F32)

    return pl.pallas_call(
        body, name=name, grid=(k // tk,),
        in_specs=[pl.BlockSpec((m, tk), lambda j: (0, j)), pl.BlockSpec((tk, n), lambda j: (j, 0))],
        out_specs=pl.BlockSpec((m, n), lambda j: (0, 0)),
        out_shape=jax.ShapeDtypeStruct((m, n), F32), compiler_params=_params("arbitrary"),
    )(at, b)


def _inproj_bwd(du_a, du_b, du_g, w_a, w_b, w_g, x, dx2, g, exchange=None, tm=256):
    s, d = x.shape
    nb = s // tm

    def body(*refs):
        ((da_ref, db_ref, dg_ref, wa_ref, wb_ref, wg_ref, x_ref, dx2_ref, g_ref), (gx_ref, gg_ref), _,
         moves) = _split_refs(refs, 9, 2, exchange)
        i = pl.program_id(0)
        if moves:
            moves.start(also=(i == 0))

        @pl.when(i == 0)
        def _():
            gg_ref[...] = jnp.zeros_like(gg_ref)

        dh = _dot_nt(da_ref[...].astype(BF16), wa_ref[...])
        dh += _dot_nt(db_ref[...].astype(BF16), wb_ref[...])
        dh += _dot_nt(dg_ref[...].astype(BF16), wg_ref[...])
        xv = x_ref[...]
        r = lax.rsqrt(jnp.mean(xv * xv, axis=-1, keepdims=True) + RMS_EPS)
        xh = xv * r
        gg_ref[...] += _colsum(dh * xh)
        dxh = dh * g_ref[...]
        gx_ref[...] = dx2_ref[...] + r * (dxh - xh * jnp.mean(dxh * xh, axis=-1, keepdims=True))
        if moves:
            moves.wait(also=(i == nb - 1))

    row = lambda w: pl.BlockSpec((tm, w), lambda i: (i, 0))
    full = lambda a: pl.BlockSpec(a.shape, lambda i: (0, 0))
    ex_in = exchange.operands if exchange else []
    ex_out = exchange.out_shapes if exchange else []
    res = pl.pallas_call(
        body, name="inproj_bwd", grid=(nb,),
        in_specs=[row(SEC), row(SEC), row(GATE_COLS), full(w_a), full(w_b), full(w_g), row(d), row(d), full(g)]
                 + [ANY] * len(ex_in),
        out_specs=[row(d), pl.BlockSpec((1, d), lambda i: (0, 0))] + [ANY] * len(ex_out),
        out_shape=[jax.ShapeDtypeStruct((s, d), F32), jax.ShapeDtypeStruct((1, d), F32)] + ex_out,
        scratch_shapes=exchange.scratch() if exchange else [],
        compiler_params=_params("arbitrary"),
    )(du_a, du_b, du_g, w_a, w_b, w_g, x, dx2, g, *ex_in)
    return res[0], res[1], list(res[2:])


def _rwkv_elementwise(ua, prev_row, first, mu, wl, w0, a0, kkw, kaw, bd):
    tm = ua.shape[0]
    rows = lax.broadcasted_iota(jnp.int32, (tm, 1), 0)
    prev = jnp.where(first, jnp.zeros_like(prev_row), prev_row)
    shifted = jnp.where(rows == 0, prev, pltpu.roll(ua, 1, 0))
    delta = shifted - ua
    us = ua + delta * mu
    r = us[:, 0:512]
    k0 = us[:, 512:1024]
    v = us[:, 1024:1536]
    lo = us[:, 1536:1664]
    gate = us[:, 1664:2176]
    lane = lax.broadcasted_iota(jnp.int32, (1, LANES), 1)
    th = jnp.tanh(lo)
    lin = jnp.where(lane < LORA, th, lo)
    ll = jnp.dot(lin.astype(BF16), wl, preferred_element_type=F32)
    sz = _sigmoid(w0 + ll[:, :512])
    e = sz * math.exp(-0.5)
    dec = jnp.exp(-e)
    a = _sigmoid(a0 + ll[:, 512:])
    kk0 = k0 * kkw
    ss = _head_sum(kk0 * kk0, bd)
    nrm = jnp.maximum(jnp.sqrt(ss), 1e-12)
    kk = kk0 / nrm
    k = k0 * (1.0 + (a - 1.0) * kaw)
    return dict(delta=delta, us=us, r=r, k0=k0, v=v, lo=lo, gate=gate, th=th, lin=lin, sz=sz, e=e, dec=dec,
                a=a, kk0=kk0, ss=ss, nrm=nrm, kk=kk, k=k)


def _rwkv_front(x, g, w_a, mu, wl, w0, a0, kkw, kaw, tm=256):
    s, d = x.shape

    def body(x_ref, g_ref, wa_ref, mu_ref, wl_ref, w0_ref, a0_ref, kkw_ref, kaw_ref,
             h_ref, ua_ref, r_ref, w_ref, k_ref, v_ref, a_ref, b_ref, gate_ref, last_row):
        i = pl.program_id(0)
        xv = x_ref[...]
        h = (xv * lax.rsqrt(jnp.mean(xv * xv, axis=-1, keepdims=True) + RMS_EPS) * g_ref[...]).astype(BF16)
        h_ref[...] = h
        ua = jnp.dot(h, wa_ref[...], preferred_element_type=F32)
        ua_ref[...] = ua

        @pl.when(i == 0)
        def _():
            last_row[...] = jnp.zeros_like(last_row)

        f = _rwkv_elementwise(ua, last_row[...], i == 0, mu_ref[...], wl_ref[...], w0_ref[...],
                              a0_ref[...], kkw_ref[...], kaw_ref[...], _head_ones())
        last_row[...] = ua[tm - 1:tm, :]
        r_ref[...] = f["r"]
        w_ref[...] = f["dec"]
        k_ref[...] = f["k"]
        v_ref[...] = f["v"]
        a_ref[...] = -f["kk"]
        b_ref[...] = f["kk"] * f["a"]
        gate_ref[...] = f["gate"]

    vec = lambda w: pl.BlockSpec((1, w), lambda i: (0, 0))
    row = lambda w: pl.BlockSpec((tm, w), lambda i: (i, 0))
    return pl.pallas_call(
        body, name="rwkv_front", grid=(s // tm,),
        in_specs=[row(d), vec(d), pl.BlockSpec(w_a.shape, lambda i: (0, 0), pipeline_mode=pl.Buffered(1)),
                  vec(SEC), pl.BlockSpec((LANES, 2 * D_HALF), lambda i: (0, 0)),
                  vec(D_HALF), vec(D_HALF), vec(D_HALF), vec(D_HALF)],
        out_specs=[row(d), row(SEC)] + [row(D_HALF)] * 7,
        out_shape=[jax.ShapeDtypeStruct((s, d), BF16), jax.ShapeDtypeStruct((s, SEC), F32)]
                  + [jax.ShapeDtypeStruct((s, D_HALF), F32)] * 7,
        scratch_shapes=[pltpu.VMEM((1, SEC), F32)],
        compiler_params=_params("arbitrary"),
    )(x, g, w_a, mu, wl, w0, a0, kkw, kaw)


SCAN_TB = 128
N_PAIRS = 4


def _pair_sum(x, left):
    s_l = jnp.sum(jnp.where(left, x, 0.0), axis=1, keepdims=True)
    s_r = jnp.sum(jnp.where(left, 0.0, x), axis=1, keepdims=True)
    return jnp.where(left, s_l, s_r)


def _pair_dot(x, row_l, row_r, left):
    s_l = jnp.sum(x * row_l, axis=1, keepdims=True)
    s_r = jnp.sum(x * row_r, axis=1, keepdims=True)
    return jnp.where(left, s_l, s_r)


def _halves(rows8):
    lane = lax.broadcasted_iota(jnp.int32, rows8.shape, 1)
    keep_left = (lane & (LANES - 1)) < HEAD
    return jnp.where(keep_left, rows8, 0.0), jnp.where(keep_left, 0.0, rows8)


def _quad_consts():
    lane = lax.broadcasted_iota(jnp.int32, (HEAD, 2 * LANES), 1)
    rowi = lax.broadcasted_iota(jnp.int32, (HEAD, 2 * LANES), 0)
    diag2 = rowi == (lane & (HEAD - 1))
    r = lax.broadcasted_iota(jnp.int32, (2 * LANES, 2 * LANES), 0) >> 6
    c = lax.broadcasted_iota(jnp.int32, (2 * LANES, 2 * LANES), 1) >> 6
    return diag2, (r == c).astype(BF16)


def _rows_to_columns(x8, diag2, bd2):
    lhs = jnp.concatenate([jnp.where(diag2, x8[i:i + 1], 0.0).astype(BF16) for i in range(SUBLANES)], axis=0)
    return jnp.dot(lhs, bd2, preferred_element_type=F32)


def _diag_rows(qtile, diag2, bd2, sub_row2):
    res = jnp.dot(qtile, bd2, preferred_element_type=F32)
    out = jnp.zeros((SUBLANES, 2 * LANES), F32)
    for i in range(SUBLANES):
        out = jnp.where(sub_row2 == i, _colsum(jnp.where(diag2, res[i * HEAD:(i + 1) * HEAD], 0.0)), out)
    return out


def _store_tile(qbuf, slot, p, i, x):
    qbuf[slot, p // 2, i * HEAD:(i + 1) * HEAD, (p % 2) * LANES:(p % 2 + 1) * LANES] = x.astype(BF16)


def _left_half():
    return lax.broadcasted_iota(jnp.int32, (HEAD, LANES), 1) < HEAD


def _split_refs(refs, n_rows, n_out, exchange):
    n_in = len(exchange.operands) if exchange else 0
    n_ex_out = len(exchange.out_shapes) if exchange else 0
    refs = list(refs)
    rows, refs = refs[:n_rows], refs[n_rows:]
    ex_in, refs = refs[:n_in], refs[n_in:]
    outs, refs = refs[:n_out], refs[n_out:]
    ex_out, refs = refs[:n_ex_out], refs[n_ex_out:]
    scratch, sems = (refs[:-3], refs[-3:]) if exchange else (refs, None)
    moves = exchange.moves(ex_in, ex_out, sems) if exchange else None
    return rows, outs, scratch, moves


def _wkv_fwd(r, w, k, a, b, v, exchange=None):
    s = r.shape[0]
    tb = SCAN_TB
    nb = s // tb

    def body(*refs):
        (r_ref, w_ref, k_ref, a_ref, b_ref, v_ref), (y_ref, st_ref), (state, vbuf, qbuf), moves = _split_refs(
            refs, 6, 2, exchange)
        g = pl.program_id(0)
        if moves:
            moves.start(also=(g == 0))

        @pl.when(g == 0)
        def _():
            state[...] = jnp.zeros_like(state)
            qbuf[...] = jnp.zeros_like(qbuf)

        left = _left_half()
        diag2, bd2 = _quad_consts()
        sub_row2 = lax.broadcasted_iota(jnp.int32, (SUBLANES, 2 * LANES), 0)
        groups = tb // SUBLANES
        quads = [slice(g2 * 2 * LANES, (g2 + 1) * 2 * LANES) for g2 in range(2)]

        def rows_of(q):
            return pl.ds(pl.multiple_of(q * SUBLANES, SUBLANES), SUBLANES)

        def v_tiles(q, slot):
            v8 = v_ref[rows_of(q), :]
            for g2 in range(2):
                vbuf[slot, g2] = _rows_to_columns(v8[:, quads[g2]], diag2, bd2)

        def chain(q, slot):
            rows8 = rows_of(q)
            a8, w8, b8, k8, r8 = (x[rows8, :] for x in (a_ref, w_ref, b_ref, k_ref, r_ref))
            pairs = [slice(p * LANES, (p + 1) * LANES) for p in range(N_PAIRS)]
            a_next = pltpu.roll(a8, SUBLANES - 1, 0)
            (a8_l, a8_r), (wa8_l, wa8_r) = _halves(a8), _halves(w8 * a_next)
            ba8 =jnp.concatenate([_pair_sum(b8[:, pr] * a_next[:, pr], left[0:SUBLANES]) for pr in pairs], axis=1)
            ka8 = jnp.concatenate([_pair_sum(k8[:, pr] * a_next[:, pr], left[0:SUBLANES]) for pr in pairs], axis=1)
            sp = [state[p] for p in range(N_PAIRS)]
            for i in range(0, SUBLANES, 2):
                r0, r1 = slice(i, i + 1), slice(i + 1, i + 2)
                sums = [(_pair_dot(sp[p], a8_l[r0, pairs[p]], a8_r[r0, pairs[p]], left),
                         _pair_dot(sp[p], wa8_l[r0, pairs[p]], wa8_r[r0, pairs[p]], left)) for p in range(N_PAIRS)]
                sa0, sa1 = [s[0] for s in sums], [s[1] for s in sums]
                for p in range(N_PAIRS):
                    pr = pairs[p]
                    inner = slice((p % 2) * LANES, (p % 2 + 1) * LANES)
                    vt0 = vbuf[slot, p // 2, i * HEAD:(i + 1) * HEAD, inner]
                    vt1 = vbuf[slot, p // 2, (i + 1) * HEAD:(i + 2) * HEAD, inner]
                    sa_next = sa1[p] + sa0[p] * ba8[r0, pr] + vt0 * ka8[r0, pr]
                    s1 = sp[p] * w8[r0, pr] + sa0[p] * b8[r0, pr] + vt0 * k8[r0, pr]
                    st_ref[q * SUBLANES + i, p] = s1
                    _store_tile(qbuf, slot, p, i, s1 * r8[r0, pr])
                    s2 = s1 * w8[r1, pr] + sa_next * b8[r1, pr] + vt1 * k8[r1, pr]
                    st_ref[q * SUBLANES + i + 1, p] = s2
                    _store_tile(qbuf, slot, p, i + 1, s2 * r8[r1, pr])
                    sp[p] = s2
            for p in range(N_PAIRS):
                state[p] = sp[p]

        def y_rows(q, slot):
            for g2 in range(2):
                y_ref[rows_of(q), quads[g2]] = _diag_rows(qbuf[slot, g2], diag2, bd2, sub_row2)

        v_tiles(0, 0)

        def two_groups(j, carry):
            q0 = 2 * j
            v_tiles(q0 + 1, 1)
            chain(q0, 0)
            y_rows(jnp.maximum(q0 - 1, 0), 1)
            v_tiles(jnp.minimum(q0 + 2, groups - 1), 0)
            chain(q0 + 1, 1)
            y_rows(q0, 0)
            return carry

        lax.fori_loop(0, groups // 2, two_groups, 0)
        y_rows(groups - 1, 1)
        if moves:
            moves.wait(also=(g == nb - 1))

    rows = pl.BlockSpec((tb, D_HALF), lambda g: (g, 0))
    ex_in = exchange.operands if exchange else []
    ex_out = exchange.out_shapes if exchange else []
    res = pl.pallas_call(
        body, name="wkv_fwd", grid=(nb,),
        in_specs=[rows] * 6 + [ANY] * len(ex_in),
        out_specs=[rows, pl.BlockSpec((tb, N_PAIRS, HEAD, LANES), lambda g: (g, 0, 0, 0))] + [ANY] * len(ex_out),
        out_shape=[jax.ShapeDtypeStruct((s, D_HALF), F32),
                   jax.ShapeDtypeStruct((s, N_PAIRS, HEAD, LANES), F32)] + ex_out,
        scratch_shapes=[pltpu.VMEM((N_PAIRS, HEAD, LANES), F32),
                        pltpu.VMEM((2, 2, SUBLANES * HEAD, 2 * LANES), F32),
                        pltpu.VMEM((2, 2, SUBLANES * HEAD, 2 * LANES), BF16)]
                       + (exchange.scratch() if exchange else []),
        compiler_params=_params("arbitrary"),
    )(r, w, k, a, b, v, *ex_in)
    return res[0], res[1], list(res[2:])


def _wkv_bwd(r, w, k, a, b, v, dy, st, exchange=None):
    s = r.shape[0]
    tb = SCAN_TB
    nb = s // tb

    def body(*refs):
        ((r_ref, w_ref, k_ref, a_ref, b_ref, v_ref, dy_ref, st_ref, before_ref),
         (dr_ref, dw_ref, dk_ref, dv_ref, da_ref, db_ref), (dstate, vbuf, qbuf, sbuf),
         moves) = _split_refs(refs, 9, 6, exchange)
        g = pl.program_id(0)
        first_block = g == nb - 1
        if moves:
            moves.start(also=(g == 0))

        @pl.when(g == 0)
        def _():
            dstate[...] = jnp.zeros_like(dstate)
            qbuf[...] = jnp.zeros_like(qbuf)

        left = _left_half()
        diag2, bd2 = _quad_consts()
        sub_row = lax.broadcasted_iota(jnp.int32, (SUBLANES, LANES), 0)
        sub_row2 = lax.broadcasted_iota(jnp.int32, (SUBLANES, 2 * LANES), 0)
        groups = tb // SUBLANES
        quads = [slice(g2 * 2 * LANES, (g2 + 1) * 2 * LANES) for g2 in range(2)]
        row_refs = (dr_ref, dw_ref, dk_ref, da_ref, db_ref)

        def rows_of(q):
            return pl.ds(pl.multiple_of(q * SUBLANES, SUBLANES), SUBLANES)

        def state_before(q, i, p):
            if i > 0:
                return st_ref[q * SUBLANES + i - 1, p]
            return jnp.where(q == 0, jnp.where(first_block, 0.0, before_ref[0, p]),
                             st_ref[jnp.maximum(q * SUBLANES - 1, 0), p])

        def column_tiles(q, slot):
            rows8 = rows_of(q)
            for kind, ref in enumerate((v_ref, dy_ref)):
                x8 = ref[rows8, :]
                for g2 in range(2):
                    vbuf[slot, kind, g2] = _rows_to_columns(x8[:, quads[g2]], diag2, bd2)
            a8 = a_ref[rows8, :]
            for i in range(SUBLANES):
                for p in range(N_PAIRS):
                    _store_tile(sbuf, 0, p, i, state_before(q, i, p) * a8[i:i + 1, p * LANES:(p + 1) * LANES])
            for g2 in range(2):
                vbuf[slot, 2, g2] = jnp.dot(sbuf[0, g2], bd2, preferred_element_type=F32)

        def chain(q, slot):
            rows8 = rows_of(q)
            a8, w8, b8, k8, r8 = (x[rows8, :] for x in (a_ref, w_ref, b_ref, k_ref, r_ref))
            b8_l, b8_r = _halves(b8)
            dsp = [dstate[p] for p in range(N_PAIRS)]
            outs = [[jnp.zeros((SUBLANES, LANES), F32) for _ in row_refs] for _ in range(N_PAIRS)]
            after = [st_ref[q * SUBLANES + SUBLANES - 1, p] for p in range(N_PAIRS)]
            for i in reversed(range(SUBLANES)):
                row = slice(i, i + 1)
                pl_ = [slice(p * LANES, (p + 1) * LANES) for p in range(N_PAIRS)]
                tile = [(p // 2, slice(i * HEAD, (i + 1) * HEAD), slice((p % 2) * LANES, (p % 2 + 1) * LANES))
                        for p in range(N_PAIRS)]
                sp = [state_before(q, i, p) for p in range(N_PAIRS)]
                dyt = [vbuf[(slot, 1) + tile[p]] for p in range(N_PAIRS)]
                ds = [dsp[p] + dyt[p] * r8[row, pl_[p]] for p in range(N_PAIRS)]
                dsa = [_pair_dot(ds[p], b8_l[row, pl_[p]], b8_r[row, pl_[p]], left) for p in range(N_PAIRS)]
                sa = [vbuf[(slot, 2) + tile[p]] for p in range(N_PAIRS)]
                for p in range(N_PAIRS):
                    ar, wr, br, kr = (x[row, pl_[p]] for x in (a8, w8, b8, k8))
                    vt = vbuf[(slot, 0) + tile[p]]
                    dsp[p] = ds[p] * wr + dsa[p] * ar
                    new = (_colsum(after[p] * dyt[p]), _colsum(ds[p] * sp[p]), _colsum(ds[p] * vt),
                           _colsum(sp[p] * dsa[p]), _colsum(ds[p] * sa[p]))
                    outs[p] = [jnp.where(sub_row == i, n, o) for n, o in zip(new, outs[p])]
                    _store_tile(qbuf, slot, p, i, ds[p] * kr)
                after = sp
            for p in range(N_PAIRS):
                dstate[p] = dsp[p]
                for ref, o in zip(row_refs, outs[p]):
                    ref[rows8, p * LANES:(p + 1) * LANES] = o

        def dv_rows(q, slot):
            for g2 in range(2):
                dv_ref[rows_of(q), quads[g2]] = _diag_rows(qbuf[slot, g2], diag2, bd2, sub_row2)

        column_tiles(groups - 1, 0)

        def two_groups(j, carry):
            q0 = groups - 1 - 2 * j
            column_tiles(q0 - 1, 1)
            chain(q0, 0)
            dv_rows(jnp.minimum(q0 + 1, groups - 1), 1)
            column_tiles(jnp.maximum(q0 - 2, 0), 0)
            chain(q0 - 1, 1)
            dv_rows(q0, 0)
            return carry

        lax.fori_loop(0, groups // 2, two_groups, 0)
        dv_rows(0, 1)
        if moves:
            moves.wait(also=(g == nb - 1))

    rows = pl.BlockSpec((tb, D_HALF), lambda g: (nb - 1 - g, 0))
    ex_in = exchange.operands if exchange else []
    ex_out = exchange.out_shapes if exchange else []
    res = pl.pallas_call(
        body, name="wkv_bwd", grid=(nb,),
        in_specs=[rows] * 7 + [pl.BlockSpec((tb, N_PAIRS, HEAD, LANES), lambda g: (nb - 1 - g, 0, 0, 0)),
                               pl.BlockSpec((1, N_PAIRS, HEAD, LANES),
                                            lambda g: (jnp.maximum((nb - 1 - g) * tb - 1, 0), 0, 0, 0))]
                 + [ANY] * len(ex_in),
        out_specs=[rows] * 6 + [ANY] * len(ex_out),
        out_shape=[jax.ShapeDtypeStruct((s, D_HALF), F32)] * 6 + ex_out,
        scratch_shapes=[pltpu.VMEM((N_PAIRS, HEAD, LANES), F32),
                        pltpu.VMEM((2, 3, 2, SUBLANES * HEAD, 2 * LANES), F32),
                        pltpu.VMEM((2, 2, SUBLANES * HEAD, 2 * LANES), BF16),
                        pltpu.VMEM((1, 2, SUBLANES * HEAD, 2 * LANES), BF16)]
                       + (exchange.scratch() if exchange else []),
        compiler_params=_params("arbitrary"),
    )(r, w, k, a, b, v, dy, st, st, *ex_in)
    return list(res[:6]), list(res[6:])


def _rwkv_post_math(y, r, k, v, gate, lw, lb, rk, bd):
    mean = _head_sum(y, bd) * (1.0 / HEAD)
    yc = y - mean
    var = _head_sum(yc * yc, bd) * (1.0 / HEAD)
    rstd = lax.rsqrt(var + LNX_EPS)
    yn = yc * rstd
    rkk = _head_sum(r * k * rk, bd)
    sg = _sigmoid(gate)
    pre = yn * lw + lb + rkk * v
    return yn, rstd, rkk, sg, pre


def _rwkv_prep_bwd(u_a, h_t, grads, mu, wl, w0, a0, kkw, kaw, tm=256):
    s = u_a.shape[0]
    nb = s // tm
    d = h_t.shape[0]

    def body(ua_ref, prev_ref, ht_ref, drs_ref, dws_ref, dks_ref, dvs_ref, das_ref, dbs_ref, drb_ref, dkb_ref, dvb_ref,
             dgt_ref, mu_ref, wl_ref, w0_ref, a0_ref, kkw_ref, kaw_ref,
             du_ref, dwa_ref, dmu_ref, dwl_ref, dw0_ref, da0_ref, dkkw_ref, dkaw_ref, carry):
        i = pl.program_id(0)

        @pl.when(i == 0)
        def _():
            carry[...] = jnp.zeros_like(carry)
            for ref in (dwa_ref, dmu_ref, dwl_ref, dw0_ref, da0_ref, dkkw_ref, dkaw_ref):
                ref[...] = jnp.zeros_like(ref)

        bd = _head_ones()
        mu_v, wl_v, kkw_v, kaw_v = mu_ref[...], wl_ref[...], kkw_ref[...], kaw_ref[...]
        f = _rwkv_elementwise(ua_ref[...], prev_ref[7:8, :], i == nb - 1, mu_v, wl_v, w0_ref[...],
                              a0_ref[...], kkw_v, kaw_v, bd)
        a, kk, k0 = f["a"], f["kk"], f["k0"]
        dk = dks_ref[...] + dkb_ref[...]
        dbs = dbs_ref[...]
        dkk = dbs * a - das_ref[...]
        da = dbs * kk + dk * k0 * kaw_v
        dk0 = dk * (1.0 + (a - 1.0) * kaw_v)
        dkaw_ref[...] += _colsum(dk * k0 * (a - 1.0))
        inv = 1.0 / f["nrm"]
        proj = _head_sum(dkk * kk, bd)
        dkk0 = jnp.where(f["ss"] > 1e-24, (dkk - kk * proj) * inv, dkk * inv)
        dk0 = dk0 + dkk0 * kkw_v
        dkkw_ref[...] += _colsum(dkk0 * k0)
        dza = da * a * (1.0 - a)
        da0_ref[...] += _colsum(dza)
        dz = -dws_ref[...] * f["dec"] * f["e"] * (1.0 - f["sz"])
        dw0_ref[...] += _colsum(dz)
        dll = jnp.concatenate([dz, dza], axis=1).astype(BF16)
        dwl_ref[...] += _dot_tn(f["lin"].astype(BF16), dll)
        dlin = _dot_nt(dll, wl_v)
        lane = lax.broadcasted_iota(jnp.int32, (1, LANES), 1)
        th = f["th"]
        dlo = jnp.where(lane < LORA, dlin * (1.0 - th * th), dlin)
        dus = jnp.concatenate([drs_ref[...] + drb_ref[...], dk0, dvs_ref[...] + dvb_ref[...], dlo, dgt_ref[...]],
                              axis=1)
        dmu_ref[...] += _colsum(dus * f["delta"])
        g1 = dus * mu_v
        rows = lax.broadcasted_iota(jnp.int32, (tm, 1), 0)
        up = jnp.where(rows == tm - 1, carry[...], pltpu.roll(g1, tm - 1, 0))
        dua = dus - g1 + up
        du_ref[...] = dua
        dwa_ref[...] += jnp.dot(ht_ref[...], dua.astype(BF16), preferred_element_type=F32)
        carry[...] = g1[0:1, :]

    rev = lambda w: pl.BlockSpec((tm, w), lambda i: (nb - 1 - i, 0))
    vec = lambda w: pl.BlockSpec((1, w), lambda i: (0, 0))
    wl_spec = pl.BlockSpec((LANES, 2 * D_HALF), lambda i: (0, 0))
    return pl.pallas_call(
        body, name="rwkv_prep_bwd", grid=(nb,),
        in_specs=[rev(SEC), pl.BlockSpec((8, SEC), lambda i: (jnp.maximum((nb - 1 - i) * (tm // 8) - 1, 0), 0)),
                  pl.BlockSpec((d, tm), lambda i: (0, nb - 1 - i))]
                 + [rev(D_HALF)] * 10 + [vec(SEC), wl_spec] + [vec(D_HALF)] * 4,
        out_specs=[rev(SEC), pl.BlockSpec((d, SEC), lambda i: (0, 0)), vec(SEC), wl_spec] + [vec(D_HALF)] * 4,
        out_shape=[jax.ShapeDtypeStruct((s, SEC), F32), jax.ShapeDtypeStruct((d, SEC), F32),
                   jax.ShapeDtypeStruct((1, SEC), F32),
                   jax.ShapeDtypeStruct((LANES, 2 * D_HALF), F32)] + [jax.ShapeDtypeStruct((1, D_HALF), F32)] * 4,
        scratch_shapes=[pltpu.VMEM((1, SEC), F32)],
        compiler_params=_params("arbitrary"),
    )(u_a, u_a, h_t, *grads, mu, wl, w0, a0, kkw, kaw)


def _tri(tm, lower):
    r = lax.broadcasted_iota(jnp.int32, (tm, tm), 0)
    c = lax.broadcasted_iota(jnp.int32, (tm, tm), 1)
    return ((r >= c) if lower else (r <= c)).astype(BF16)


def _head_rms(x, g, bd):
    rinv = lax.rsqrt(_head_sum(x * x, bd) * (1.0 / HEAD) + RMS_EPS)
    xh = x * rinv
    return xh, rinv, xh * g


def _fox_front(h, w_b, fb, qg, kg, tm=256):
    s, d = h.shape

    def body(h_ref, wb_ref, fb_ref, qg_ref, kg_ref, ub_ref, q_ref, k_ref, v_ref, cc_ref, cr_ref, carry):
        i = pl.program_id(0)

        @pl.when(i == 0)
        def _():
            carry[...] = jnp.zeros_like(carry)

        ub_ref[...] = jnp.dot(h_ref[...], wb_ref[...], preferred_element_type=F32)
        bd = _head_ones()
        _, _, qn = _head_rms(ub_ref[:, 0:512], qg_ref[...], bd)
        _, _, kn = _head_rms(ub_ref[:, 512:1024], kg_ref[...], bd)
        q_ref[...] = (qn * ATT_SCALE).astype(BF16)
        k_ref[...] = kn.astype(BF16)
        v_ref[...] = ub_ref[:, 1024:1536].astype(BF16)
        lane = lax.broadcasted_iota(jnp.int32, (1, LANES), 1)
        logf = jnp.where(lane < N_HEADS, _log_sigmoid(ub_ref[:, 2048:2176] + fb_ref[...]), 0.0)
        cum = _exact_dot(logf, _tri(tm, True), ones_first=True) + carry[...]
        for h in range(N_HEADS):
            cc_ref[h] = jnp.broadcast_to(cum[:, h:h + 1], (tm, LANES))
        cr_ref[...] = jnp.transpose(cum)[0:N_HEADS, :]
        carry[...] = cum[tm - 1:tm, :]

    blk = pl.BlockSpec((tm, D_HALF), lambda i: (i, 0))
    return pl.pallas_call(
        body, name="fox_front", grid=(s // tm,),
        in_specs=[pl.BlockSpec((tm, d), lambda i: (i, 0)),
                  pl.BlockSpec(w_b.shape, lambda i: (0, 0), pipeline_mode=pl.Buffered(1)),
                  pl.BlockSpec((1, LANES), lambda i: (0, 0)),
                  pl.BlockSpec((1, D_HALF), lambda i: (0, 0)), pl.BlockSpec((1, D_HALF), lambda i: (0, 0))],
        out_specs=[pl.BlockSpec((tm, SEC), lambda i: (i, 0)), blk, blk, blk,
                   pl.BlockSpec((N_HEADS, tm, LANES), lambda i: (0, i, 0)), pl.BlockSpec((N_HEADS, tm), lambda i: (0, i))],
        out_shape=[jax.ShapeDtypeStruct((s, SEC), F32)] + [jax.ShapeDtypeStruct((s, D_HALF), BF16)] * 3
                  + [jax.ShapeDtypeStruct((N_HEADS, s, LANES), F32), jax.ShapeDtypeStruct((N_HEADS, s), F32)],
        scratch_shapes=[pltpu.VMEM((1, LANES), F32)],
        compiler_params=_params("arbitrary"),
    )(h, w_b, fb, qg, kg)


ATT_T = 256


def _tiles(nblk, by_query):
    if by_query:
        pairs = [(i, j) for i in range(nblk) for j in range(i + 1)]
    else:
        pairs = [(i, j) for j in range(nblk) for i in range(j, nblk)]
    return (jnp.asarray([p[0] for p in pairs], jnp.int32), jnp.asarray([p[1] for p in pairs], jnp.int32))


def _attn_fwd(q, k, v, cc, cr):
    s = q.shape[0]
    t = ATT_T
    nblk = s // t

    def body(qi_ref, kj_ref, q_ref, k_ref, v_ref, cc_ref, cr_ref, o_ref, lse_ref, m_sc, l_sc, acc_sc):
        i = qi_ref[pl.program_id(0)]
        j = kj_ref[pl.program_id(0)]

        @pl.when(j == 0)
        def _():
            m_sc[...] = jnp.full_like(m_sc, NEG)
            l_sc[...] = jnp.zeros_like(l_sc)
            acc_sc[...] = jnp.zeros_like(acc_sc)

        def tile(on_diagonal):
            causal = _causal_tile(t) if on_diagonal else None
            left = lax.broadcasted_iota(jnp.int32, (1, LANES), 1) < HEAD
            for p in range(N_PAIRS):
                lanes = slice(p * LANES, (p + 1) * LANES)
                q2, k2, v2 = q_ref[:, lanes], k_ref[:, lanes], v_ref[:, lanes]
                acc2 = acc_sc[:, lanes]
                for e in range(2):
                    h = 2 * p + e
                    msk = left if e == 0 else jnp.logical_not(left)
                    sc = _dot_nt(jnp.where(msk, q2, jnp.zeros_like(q2)), k2)
                    sc = sc + (_wide(cc_ref[h]) - cr_ref[h:h + 1, :])
                    if on_diagonal:
                        sc = jnp.where(causal, sc, NEG)
                    m_prev = m_sc[h]
                    m_new = jnp.maximum(m_prev, jnp.max(sc, axis=1, keepdims=True))
                    alpha = jnp.exp(m_prev - m_new)
                    pm = jnp.exp(sc - _wide(m_new))
                    l_sc[h] = alpha * l_sc[h] + jnp.sum(pm, axis=1, keepdims=True)
                    m_sc[h] = m_new
                    pv = jnp.dot(pm.astype(BF16), v2, preferred_element_type=F32)
                    acc2 = jnp.where(msk, alpha * acc2 + pv, acc2)
                acc_sc[:, lanes] = acc2

        pl.when(j < i)(functools.partial(tile, False))
        pl.when(j == i)(functools.partial(tile, True))

        @pl.when(j == i)
        def _():
            left = lax.broadcasted_iota(jnp.int32, (1, LANES), 1) < HEAD
            for p in range(N_PAIRS):
                lanes = slice(p * LANES, (p + 1) * LANES)
                inv = jnp.where(left, 1.0 / l_sc[2 * p], 1.0 / l_sc[2 * p + 1])
                o_ref[:, lanes] = acc_sc[:, lanes] * inv
            for h in range(N_HEADS):
                lse_ref[h] = m_sc[h] + jnp.log(l_sc[h])

    qi, kj = _tiles(nblk, by_query=True)
    qblk = pl.BlockSpec((t, D_HALF), lambda n, qi, kj: (qi[n], 0))
    kblk = pl.BlockSpec((t, D_HALF), lambda n, qi, kj: (kj[n], 0))
    qrep = pl.BlockSpec((N_HEADS, t, LANES), lambda n, qi, kj: (0, qi[n], 0))
    return pl.pallas_call(
        body, name="fox_attn_fwd",
        grid_spec=pltpu.PrefetchScalarGridSpec(
            num_scalar_prefetch=2, grid=(qi.shape[0],),
            in_specs=[qblk, kblk, kblk, qrep, pl.BlockSpec((N_HEADS, t), lambda n, qi, kj: (0, kj[n]))],
            out_specs=[qblk, qrep],
            scratch_shapes=[pltpu.VMEM((N_HEADS, t, LANES), F32), pltpu.VMEM((N_HEADS, t, LANES), F32),
                            pltpu.VMEM((t, D_HALF), F32)]),
        out_shape=[jax.ShapeDtypeStruct((s, D_HALF), F32), jax.ShapeDtypeStruct((N_HEADS, s, LANES), F32)],
        compiler_params=_params("arbitrary"),
    )(qi, kj, q, k, v, cc, cr)


def _causal_tile(t):
    return lax.broadcasted_iota(jnp.int32, (t, t), 0) >= lax.broadcasted_iota(jnp.int32, (t, t), 1)


def _wide(x):
    return jnp.concatenate([x, x], axis=1)


def _attn_probs(q2, k2, v2, do2, msk, causal, bias, lse_rows):
    zero = jnp.zeros_like(q2)
    qh = jnp.where(msk, q2, zero)
    doh = jnp.where(msk, do2, zero)
    sc = _dot_nt(qh, k2) + bias
    if causal is not None:
        sc = jnp.where(causal, sc, NEG)
    pm = jnp.exp(sc - _wide(lse_rows))
    dp = _dot_nt(doh, v2)
    return qh, doh, pm, dp


def _attn_bwd_rowdot(q, k, v, do, lse, cc, cr):
    s = q.shape[0]
    t = ATT_T
    nblk = s // t

    def body(qi_ref, kj_ref, q_ref, k_ref, v_ref, do_ref, lse_ref, cc_ref, cr_ref, dd_ref, acc):
        i = qi_ref[pl.program_id(0)]
        j = kj_ref[pl.program_id(0)]

        @pl.when(j == 0)
        def _():
            acc[...] = jnp.zeros_like(acc)

        def tile(on_diagonal):
            causal = _causal_tile(t) if on_diagonal else None
            left = lax.broadcasted_iota(jnp.int32, (1, LANES), 1) < HEAD
            for p in range(N_PAIRS):
                lanes = slice(p * LANES, (p + 1) * LANES)
                q2, k2, v2, do2 = q_ref[:, lanes], k_ref[:, lanes], v_ref[:, lanes], do_ref[:, lanes]
                for e in range(2):
                    h = 2 * p + e
                    msk = left if e == 0 else jnp.logical_not(left)
                    bias = _wide(cc_ref[h]) - cr_ref[h:h + 1, :]
                    _, _, pm, dp = _attn_probs(q2, k2, v2, do2, msk, causal, bias, lse_ref[h])
                    acc[h] += jnp.sum(pm * dp, axis=1, keepdims=True)

        pl.when(j < i)(functools.partial(tile, False))
        pl.when(j == i)(functools.partial(tile, True))

        @pl.when(j == i)
        def _():
            dd_ref[...] = acc[...]

    qi, kj = _tiles(nblk, by_query=True)
    qblk = pl.BlockSpec((t, D_HALF), lambda n, qi, kj: (qi[n], 0))
    qcol = pl.BlockSpec((N_HEADS, t, LANES), lambda n, qi, kj: (0, qi[n], 0))
    kblk = pl.BlockSpec((t, D_HALF), lambda n, qi, kj: (kj[n], 0))
    return pl.pallas_call(
        body, name="fox_attn_rowdot",
        grid_spec=pltpu.PrefetchScalarGridSpec(
            num_scalar_prefetch=2, grid=(qi.shape[0],),
            in_specs=[qblk, kblk, kblk, qblk, qcol, qcol, pl.BlockSpec((N_HEADS, t), lambda n, qi, kj: (0, kj[n]))],
            out_specs=qcol, scratch_shapes=[pltpu.VMEM((N_HEADS, t, LANES), F32)]),
        out_shape=jax.ShapeDtypeStruct((N_HEADS, s, LANES), F32),
        compiler_params=_params("arbitrary"),
    )(qi, kj, q, k, v, do, lse, cc, cr)


def _attn_bwd(q, k, v, do, lse, dd, cc, cr):
    s = q.shape[0]
    t = ATT_T
    nblk = s // t

    def body(qi_ref, kj_ref, q_ref, k_ref, v_ref, do_ref, lse_ref, dd_ref, cc_ref, cr_ref,
             dq_ref, dk_ref, dv_ref, dcr_ref, dk_sc, dv_sc, dcr_sc):
        i = qi_ref[pl.program_id(0)]
        j = kj_ref[pl.program_id(0)]

        @pl.when(pl.program_id(0) == 0)
        def _():
            dq_ref[...] = jnp.zeros_like(dq_ref)

        @pl.when(i == j)
        def _():
            dk_sc[...] = jnp.zeros_like(dk_sc)
            dv_sc[...] = jnp.zeros_like(dv_sc)
            dcr_sc[...] = jnp.zeros_like(dcr_sc)

        def tile(on_diagonal):
            causal = _causal_tile(t) if on_diagonal else None
            left = lax.broadcasted_iota(jnp.int32, (1, LANES), 1) < HEAD
            qrows = pl.ds(pl.multiple_of(i * t, t), t)
            for p in range(N_PAIRS):
                lanes = slice(p * LANES, (p + 1) * LANES)
                q2, k2, v2, do2 = q_ref[:, lanes], k_ref[:, lanes], v_ref[:, lanes], do_ref[:, lanes]
                zero = jnp.zeros_like(q2)
                dq2 = jnp.zeros((t, LANES), F32)
                dk2 = jnp.zeros((t, LANES), F32)
                dv2 = jnp.zeros((t, LANES), F32)
                for e in range(2):
                    h = 2 * p + e
                    msk = left if e == 0 else jnp.logical_not(left)
                    bias = _wide(cc_ref[h]) - cr_ref[h:h + 1, :]
                    qh, doh, pm, dp = _attn_probs(q2, k2, v2, do2, msk, causal, bias, lse_ref[h])
                    dsc = pm * (dp - _wide(dd_ref[h]))
                    dsb = dsc.astype(BF16)
                    dv2 += _dot_tn(pm.astype(BF16), doh)
                    dk2 += _dot_tn(dsb, qh)
                    dq2 += jnp.dot(dsb, jnp.where(msk, k2, zero), preferred_element_type=F32)
                    dcr_sc[h:h + 1, :] += -_colsum(dsc)
                dq_ref[qrows, lanes] += dq2 * ATT_SCALE
                dk_sc[:, lanes] += dk2
                dv_sc[:, lanes] += dv2

        pl.when(i > j)(functools.partial(tile, False))
        pl.when(i == j)(functools.partial(tile, True))

        @pl.when(i == nblk - 1)
        def _():
            dk_ref[...] = dk_sc[...]
            dv_ref[...] = dv_sc[...]
            dcr_ref[...] = dcr_sc[...]

    qi, kj = _tiles(nblk, by_query=False)
    qblk = pl.BlockSpec((t, D_HALF), lambda n, qi, kj: (qi[n], 0))
    qcol = pl.BlockSpec((N_HEADS, t, LANES), lambda n, qi, kj: (0, qi[n], 0))
    kblk = pl.BlockSpec((t, D_HALF), lambda n, qi, kj: (kj[n], 0))
    krow = pl.BlockSpec((N_HEADS, t), lambda n, qi, kj: (0, kj[n]))
    return pl.pallas_call(
        body, name="fox_attn_bwd",
        grid_spec=pltpu.PrefetchScalarGridSpec(
            num_scalar_prefetch=2, grid=(qi.shape[0],),
            in_specs=[qblk, kblk, kblk, qblk, qcol, qcol, qcol, krow],
            out_specs=[pl.BlockSpec((s, D_HALF), lambda n, qi, kj: (0, 0)), kblk, kblk, krow],
            scratch_shapes=[pltpu.VMEM((t, D_HALF), F32), pltpu.VMEM((t, D_HALF), F32), pltpu.VMEM((N_HEADS, t), F32)]),
        out_shape=[jax.ShapeDtypeStruct((s, D_HALF), F32)] * 3 + [jax.ShapeDtypeStruct((N_HEADS, s), F32)],
        compiler_params=_params("arbitrary"),
    )(qi, kj, q, k, v, do, lse, dd, cc, cr)


def _fox_prep_bwd(u_b, h_t, dq, dk, dv, dgate, dcum, fb, qg, kg, tm=256):
    s = u_b.shape[0]
    nb = s // tm
    d = h_t.shape[0]

    def body(ub_ref, ht_ref, dq_ref, dk_ref, dv_ref, dg_ref, dc_ref, fb_ref, qg_ref, kg_ref,
             du_ref, dwb_ref, dqg_ref, dkg_ref, dfb_ref, carry):
        i = pl.program_id(0)

        @pl.when(i == 0)
        def _():
            carry[...] = jnp.zeros_like(carry)
            dwb_ref[...] = jnp.zeros_like(dwb_ref)
            dqg_ref[...] = jnp.zeros_like(dqg_ref)
            dkg_ref[...] = jnp.zeros_like(dkg_ref)
            dfb_ref[...] = jnp.zeros_like(dfb_ref)

        bd = _head_ones()
        for lo, g_ref, d_ref, dgain_ref in ((0, qg_ref, dq_ref, dqg_ref), (512, kg_ref, dk_ref, dkg_ref)):
            gain = g_ref[...]
            xh, rinv, _ = _head_rms(ub_ref[:, lo:lo + 512], gain, bd)
            dn = d_ref[...]
            dgain_ref[...] += _colsum(dn * xh)
            dxh = dn * gain
            du_ref[:, lo:lo + 512] = rinv * (dxh - xh * (_head_sum(dxh * xh, bd) * (1.0 / HEAD)))
        du_ref[:, 1024:1536] = dv_ref[...]
        du_ref[:, 1536:2048] = dg_ref[...]
        lane = lax.broadcasted_iota(jnp.int32, (1, LANES), 1)
        dc = dc_ref[...]
        dlogf = _exact_dot(dc, _tri(tm, False), ones_first=True) + carry[...]
        carry[...] += _colsum(dc)
        fl = ub_ref[:, 2048:2176] + fb_ref[...]
        dfl = jnp.where(lane < N_HEADS, dlogf * (1.0 - _sigmoid(fl)), 0.0)
        du_ref[:, 2048:2176] = dfl
        dfb_ref[...] += _colsum(dfl)
        dwb_ref[...] += jnp.dot(ht_ref[...], du_ref[...].astype(BF16), preferred_element_type=F32)

    rev = lambda w: pl.BlockSpec((tm, w), lambda i: (nb - 1 - i, 0))
    vec = lambda w: pl.BlockSpec((1, w), lambda i: (0, 0))
    return pl.pallas_call(
        body, name="fox_prep_bwd", grid=(nb,),
        in_specs=[rev(SEC), pl.BlockSpec((d, tm), lambda i: (0, nb - 1 - i))] + [rev(D_HALF)] * 4
                 + [rev(LANES), vec(LANES), vec(D_HALF), vec(D_HALF)],
        out_specs=[rev(SEC), pl.BlockSpec((d, SEC), lambda i: (0, 0)), vec(D_HALF), vec(D_HALF), vec(LANES)],
        out_shape=[jax.ShapeDtypeStruct((s, SEC), F32), jax.ShapeDtypeStruct((d, SEC), F32),
                   jax.ShapeDtypeStruct((1, D_HALF), F32), jax.ShapeDtypeStruct((1, D_HALF), F32),
                   jax.ShapeDtypeStruct((1, LANES), F32)],
        scratch_shapes=[pltpu.VMEM((1, LANES), F32)],
        compiler_params=_params("arbitrary"),
    )(u_b, h_t, dq, dk, dv, dgate, dcum, fb, qg, kg)


def _merge(y, r, k, v, gate_a, o, u_b, h, x, tgt, w_g, wa, wb, wo, fg, lw, lb, rk, tm=256):
    s, d = x.shape

    def body(y_ref, r_ref, k_ref, v_ref, ga_ref, o_ref, gb_ref, h_ref, x_ref, t_ref, wg_ref, wa_ref, wb_ref, wo_ref,
             fg_ref, lw_ref, lb_ref, rk_ref,
             dx2_ref, dy_ref, drb_ref, dkb_ref, dvb_ref, dga_ref, do_ref, dgb_ref, dug_ref,
             dwa_ref, dwb_ref, dwo_ref, dfg_ref, loss_ref, dlw_ref, dlb_ref, drk_ref):
        i = pl.program_id(0)

        @pl.when(i == 0)
        def _():
            for ref in (dwa_ref, dwb_ref, dwo_ref, dfg_ref, loss_ref, dlw_ref, dlb_ref, drk_ref):
                ref[...] = jnp.zeros_like(ref)

        bd = _head_ones()
        wa_v, wb_v, wo_v, fg_v = wa_ref[...], wb_ref[...], wo_ref[...], fg_ref[...]
        rv, kv, vv, ga, lw_v, rk_v = r_ref[...], k_ref[...], v_ref[...], ga_ref[...], lw_ref[...], rk_ref[...]
        yn, rstd, rkk, sga, pre = _rwkv_post_math(y_ref[...], rv, kv, vv, ga, lw_v, lb_ref[...], rk_v, bd)
        silu_a = ga * sga
        gb, ov = gb_ref[...], o_ref[...]
        sgb = _sigmoid(gb)
        silu_b = gb * sgb
        ma = (pre * silu_a).astype(BF16)
        mb = (ov * silu_b).astype(BF16)
        ya = jnp.dot(ma, wa_v, preferred_element_type=F32)
        yb = jnp.dot(mb, wb_v, preferred_element_type=F32)
        ug = jnp.dot(h_ref[...], wg_ref[...], preferred_element_type=F32)
        sa = _sigmoid(ug[:, 0:d])
        sb = _sigmoid(ug[:, d:2 * d])
        merged = (sa * ya + sb * yb).astype(BF16)
        x2 = x_ref[...] + jnp.dot(merged, wo_v, preferred_element_type=F32)
        r2 = lax.rsqrt(jnp.mean(x2 * x2, axis=-1, keepdims=True) + RMS_EPS)
        x2h = x2 * r2
        err = x2h * fg_v - t_ref[...]
        loss_ref[...] += _colsum(err * err)
        dyo = err * (1.0 / d)
        dfg_ref[...] += _colsum(dyo * x2h)
        dx2h = dyo * fg_v
        dx2 = r2 * (dx2h - x2h * jnp.mean(dx2h * x2h, axis=-1, keepdims=True))
        dx2_ref[...] = dx2
        dx2b = dx2.astype(BF16)
        dmerged = _dot_nt(dx2b, wo_v)
        dwo_ref[...] += _dot_tn(merged, dx2b)
        dya = dmerged * sa
        dyb = dmerged * sb
        dug_ref[:, 0:d] = dya * ya * (1.0 - sa)
        dug_ref[:, d:2 * d] = dyb * yb * (1.0 - sb)
        dyab = dya.astype(BF16)
        dybb = dyb.astype(BF16)
        dwa_ref[...] += _dot_tn(ma, dyab)
        dwb_ref[...] += _dot_tn(mb, dybb)
        dmb = _dot_nt(dybb, wb_v)
        do_ref[...] = (dmb * silu_b).astype(BF16)
        dgb_ref[...] = dmb * ov * (sgb * (1.0 + gb * (1.0 - sgb)))
        dma = _dot_nt(dyab, wa_v)
        dga_ref[...] = dma * pre * (sga * (1.0 + ga * (1.0 - sga)))
        dpre = dma * silu_a
        dlw_ref[...] += _colsum(dpre * yn)
        dlb_ref[...] += _colsum(dpre)
        dyn = dpre * lw_v
        m1 = _head_sum(dyn, bd) * (1.0 / HEAD)
        m2 = _head_sum(dyn * yn, bd) * (1.0 / HEAD)
        dy_ref[...] = rstd * (dyn - m1 - yn * m2)
        dvb_ref[...] = dpre * rkk
        drkk = _head_sum(dpre * vv, bd)
        drb_ref[...] = drkk * kv * rk_v
        dkb_ref[...] = drkk * rv * rk_v
        drk_ref[...] += _colsum(drkk * rv * kv)

    row = lambda w: pl.BlockSpec((tm, w), lambda i: (i, 0))
    full = lambda a: pl.BlockSpec(a.shape, lambda i: (0, 0))
    once = lambda a: pl.BlockSpec(a.shape, lambda i: (0, 0), pipeline_mode=pl.Buffered(1))
    half = jax.ShapeDtypeStruct((s, D_HALF), F32)
    fshape = lambda a: jax.ShapeDtypeStruct(a.shape, F32)
    return pl.pallas_call(
        body, name="merge_fwd_bwd", grid=(s // tm,),
        in_specs=[row(D_HALF)] * 6 + [pl.BlockSpec((tm, D_HALF), lambda i: (i, 3)), row(d), row(d), row(d),
                                      once(w_g), once(wa), once(wb), once(wo), full(fg), full(lw), full(lb), full(rk)],
        out_specs=[row(d)] + [row(D_HALF)] * 7 + [row(GATE_COLS), full(wa), full(wb), full(wo), full(fg), full(fg),
                                                   full(lw), full(lb), full(rk)],
        out_shape=[jax.ShapeDtypeStruct((s, d), F32)] + [half] * 5 + [jax.ShapeDtypeStruct((s, D_HALF), BF16), half,
                                                                    jax.ShapeDtypeStruct((s, GATE_COLS), F32),
                                                                    fshape(wa), fshape(wb), fshape(wo), fshape(fg),
                                                                    fshape(fg), fshape(lw), fshape(lb), fshape(rk)],
        compiler_params=_params("arbitrary"),
    )(y, r, k, v, gate_a, o, u_b, h, x, tgt, w_g, wa, wb, wo, fg, lw, lb, rk)


def _lora_weight(w_up, a_up):
    z = jnp.zeros((LORA, D_HALF), w_up.dtype)
    return jnp.concatenate([jnp.concatenate([w_up, z], axis=1), jnp.concatenate([z, a_up], axis=1)], axis=0)


def _device_grads(x, tgt, p, w_a, w_up, a_up, late_weights, fwd_exchange=None, bwd_exchange=None, tail_exchange=None):
    wl = _lora_weight(w_up, a_up)
    rk = p["r_k"].reshape(1, D_HALF)
    fb = jnp.pad(p["f_bias"], ((0, 0), (0, LANES - N_HEADS)))
    qg = jnp.tile(p["q_norm_g"], (1, N_HEADS))
    kg = jnp.tile(p["k_norm_g"], (1, N_HEADS))
    fg = p["final_norm_g"].reshape(1, D_MODEL)
    mixer = (p["shift_mu"], wl, p["w0"], p["a0"], p["k_k"], p["k_a"])

    h, u_a, r, dec, k, v, av, bv, gate_a = _rwkv_front(x, p["norm_g"], w_a, *mixer)
    y, st, arrived = _wkv_fwd(r, dec, k, av, bv, v, fwd_exchange)

    w_b, w_g, w_out_a, w_out_b, w_out = late_weights(arrived)
    u_b, q, kn, vb, cc, cr = _fox_front(h, w_b, fb, qg, kg)
    o, lse = _attn_fwd(q, kn, vb, cc, cr)

    (dx2, dy, dr_b, dk_b, dv_b, dgate_a, do, dgate_b, du_g, dwa, dwb, dwo, dfg, loss_vec, dlw, dlb, drk) = _merge(
        y, r, k, v, gate_a, o, u_b, h, x, tgt, w_g, w_out_a, w_out_b, w_out, fg, p["lnx_w"], p["lnx_b"], rk)

    dd = _attn_bwd_rowdot(q, kn, vb, do, lse, cc, cr)
    dq, dk_att, dv_att, dcr = _attn_bwd(q, kn, vb, do, lse, dd, cc, cr)
    dcum = jnp.pad(dcr.T, ((0, 0), (0, LANES - N_HEADS)))
    h_t = h.T
    du_b, dw_b, dqg, dkg, dfb = _fox_prep_bwd(u_b, h_t, dq, dk_att, dv_att, dgate_b, dcum, fb, qg, kg)
    dw_g = _matmul_tn_acc(h_t, du_g, "dw_gate")

    scan_grads, sent = _wkv_bwd(r, dec, k, av, bv, v, dy, st,
                                bwd_exchange(dw_b, dw_g, dwa, dwb, dwo) if bwd_exchange else None)
    du_a, dw_a, dmu, dwl, dw0, da0, dkkw, dkaw = _rwkv_prep_bwd(
        u_a, h_t, (*scan_grads, dr_b, dk_b, dv_b, dgate_a), *mixer)
    dw_up, da_up = dwl[:LORA, :D_HALF], dwl[LORA:, D_HALF:]
    sent_last = _run_on_sequencer(tail_exchange(dw_a, dw_up, da_up), "scatter_tail", 1) if tail_exchange else []
    grad_x, dnorm_g, _ = _inproj_bwd(du_a, du_b, du_g, w_a, w_b, w_g, x, dx2, p["norm_g"])

    grads = dict(
        norm_g=dnorm_g, w_in=(dw_a, dw_b, dw_g), shift_mu=dmu,
        w_lora_up=dw_up, w0=dw0, a_lora_up=da_up, a0=da0, k_k=dkkw, k_a=dkaw,
        r_k=drk.reshape(1, N_HEADS, HEAD), lnx_w=dlw, lnx_b=dlb, f_bias=dfb[:, :N_HEADS],
        q_norm_g=dqg.reshape(N_HEADS, HEAD).sum(axis=0, keepdims=True),
        k_norm_g=dkg.reshape(N_HEADS, HEAD).sum(axis=0, keepdims=True),
        w_out_a=dwa, w_out_b=dwb, w_out=dwo, final_norm_g=dfg.reshape(D_MODEL))
    return loss_vec, grad_x, grads, sent, sent_last


CHIP_FLIPS = ((1, 0), (0, 1), (1, 1))
ANY = pl.BlockSpec(memory_space=pl.ANY)


def _position():
    return lax.axis_index("x"), lax.axis_index("y"), lax.axis_index("c")


def _flip(v, f):
    return 1 - v if f else v


def _both(a, b):
    if a is None:
        return b
    return a if b is None else jnp.logical_and(a, b)


def _when(cond, fn):
    if cond is None:
        fn()
    else:
        pl.when(cond)(fn)


class _Moves:
    def __init__(self, send_sems, recv_sems, local_sems):
        self.send_sems, self.recv_sems, self.local_sems = send_sems, recv_sems, local_sems
        self.remote, self.local = [], []

    def send(self, src, dst, peer, landing, send_if=None, recv_if=None, first=False):
        k = len(self.remote)
        sems = dict(send_sem=self.send_sems.at[k], recv_sem=self.recv_sems.at[k], device_id=peer, device_id_type=MESH)
        out = pltpu.make_async_remote_copy(src_ref=src, dst_ref=dst, **sems)
        arrival = pltpu.make_async_remote_copy(src_ref=src, dst_ref=landing, **sems)
        self.remote.append((out, arrival, send_if, recv_if, first))

    def copy(self, src, dst, cond=None):
        cp = pltpu.make_async_copy(src, dst, self.local_sems.at[len(self.local)])
        self.local.append((cp, cond))

    def start(self, also=None):
        for cp, cond in self.local:
            _when(_both(also, cond), cp.start)
        for out, _, send_if, _, _ in self.remote:
            _when(_both(also, send_if), out.start)

    def wait_arrivals(self, also=None, first=None):
        for _, arrival, _, recv_if, is_first in self.remote:
            if first is None or first == is_first:
                _when(_both(also, recv_if), arrival.wait_recv)

    def wait_sent(self, also=None):
        for out, _, send_if, _, _ in self.remote:
            _when(_both(also, send_if), out.wait_send)
        for cp, cond in self.local:
            _when(_both(also, cond), cp.wait)

    def wait(self, also=None):
        self.wait_arrivals(also)
        self.wait_sent(also)


class _Exchange:
    def __init__(self, operands, out_shapes, n_remote, n_local, build, relays=None, in_place=(), n_staging=0):
        self.operands, self.out_shapes = list(operands), list(out_shapes)
        self.n_remote, self.n_local, self.build = n_remote, n_local, build
        self.relays, self.in_place = relays, in_place
        self.n_staging = n_staging

    def scratch(self):
        return [pltpu.SemaphoreType.DMA((self.n_remote,)), pltpu.SemaphoreType.DMA((self.n_remote,)),
                pltpu.SemaphoreType.DMA((max(self.n_local, 1),))]

    def moves(self, in_refs, out_refs, sems):
        mv = _Moves(*sems)
        self.build(mv, in_refs, out_refs)
        return mv


def _run_on_sequencer(exchange, name, collective_id):
    ins = [jax.new_ref(a, memory_space=pltpu.MemorySpace.HBM) for a in exchange.operands]
    outs = [ins[i] if i in exchange.in_place else jax.empty_ref(s, memory_space=pltpu.MemorySpace.HBM)
            for i, s in enumerate(exchange.out_shapes)]
    forward, to_sibling = exchange.relays or (None, None)
    relay_scratch = [pltpu.SemaphoreType.DMA((stage[0],)) for stage in (forward, to_sibling) if stage for _ in range(2)]

    def launch(*sems):
        x, y, c = _position()
        peers = [(_flip(x, fx), _flip(y, fy), c) for fx, fy in CHIP_FLIPS] + ([(x, y, 1 - c)] if to_sibling else [])
        barrier = pltpu.get_barrier_semaphore()
        for peer in peers:
            pl.semaphore_signal(barrier, inc=1, device_id=peer, device_id_type=MESH)
        pl.semaphore_wait(barrier, len(peers))
        moves = exchange.moves(ins, outs, sems[:3])
        moves.start()
        later = []
        if forward:
            onward = _Moves(sems[3], sems[4], None)
            forward[1](onward, ins, outs)
            moves.wait_arrivals(first=True)
            onward.start()
            moves.wait_arrivals(first=False)
            onward.wait_arrivals()
            later.append(onward)
        else:
            moves.wait_arrivals()
        if to_sibling:
            passed = _Moves(*sems[-2:], None)
            to_sibling[1](passed, ins, outs)
            passed.start()
            passed.wait_arrivals()
            later.append(passed)
        for mv in later + [moves]:
            mv.wait_sent()

    pl.kernel(launch, mesh=plsc.ScalarSubcoreMesh(axis_name="sequencer", num_cores=1), name=name,
              scratch_types=tuple(exchange.scratch() + relay_scratch),
              compiler_params=pltpu.CompilerParams(collective_id=collective_id))()
    return [o[...] for o in outs[:len(outs) - exchange.n_staging]]


def _row_major_copy(a, name):
    r, c = a.shape
    tr = _row_tile(r)

    def body(a_ref, o_ref):
        o_ref[...] = a_ref[...]

    blk = pl.BlockSpec((tr, c), lambda i: (i, 0))
    return pl.pallas_call(body, name=name, grid=(r // tr,), in_specs=[blk], out_specs=blk,
                          out_shape=jax.ShapeDtypeStruct(a.shape, a.dtype), compiler_params=_params("parallel"))(a)


def _is_chip(x, y, chip):
    return jnp.logical_and(x == chip // 2, y == chip % 2)


def _gather_exchange(from_chip, from_all, split=()):
    n1, n2 = len(from_chip), len(from_all)
    near = CHIP_FLIPS[:2]

    def quarters(t, c, first, count=1):
        n = from_chip[t][1].shape[0] // 4
        return pl.ds((2 * c + first) * n, count * n)

    def build(mv, ins, outs):
        x, y, c = _position()
        me = 2 * x + y
        for t, (chip, _) in enumerate(from_chip):
            if t not in split:
                mv.copy(ins[t], outs[t], cond=_is_chip(x, y, chip))
        for t in range(n2):
            mv.copy(ins[n1 + t], outs[n1 + t].at[me])
        for t in split:
            for first in (True, False):
                for f, (fx, fy) in enumerate(near):
                    px, py = _flip(x, fx), _flip(y, fy)
                    part = quarters(t, c, f if first else 1 - f)
                    mv.send(ins[t].at[part], outs[t].at[part], (px, py, c), landing=outs[t].at[part], first=first,
                            send_if=_is_chip(x, y, from_chip[t][0]), recv_if=_is_chip(px, py, from_chip[t][0]))
        for fx, fy in CHIP_FLIPS:
            px, py = _flip(x, fx), _flip(y, fy)
            peer = (px, py, c)
            for t, (chip, _) in enumerate(from_chip):
                if t not in split:
                    mv.send(ins[t], outs[t], peer, landing=outs[t],
                            send_if=_is_chip(x, y, chip), recv_if=_is_chip(px, py, chip))
            for t in range(n2):
                mv.send(ins[n1 + t], outs[n1 + t].at[me], peer, landing=outs[n1 + t].at[2 * px + py])

    def forward(mv, ins, outs):
        x, y, c = _position()
        for t in split:
            chip = from_chip[t][0]
            for f, (fx, fy) in enumerate(near):
                gx, gy = near[1 - f]
                part = quarters(t, c, f)
                mv.send(outs[t].at[part], outs[t].at[part], (_flip(x, gx), _flip(y, gy), c), landing=outs[t].at[part],
                        send_if=_is_chip(_flip(x, fx), _flip(y, fy), chip), recv_if=_is_chip(1 - x, 1 - y, chip))

    def to_sibling(mv, ins, outs):
        x, y, c = _position()
        for t in split:
            came = jnp.logical_not(_is_chip(x, y, from_chip[t][0]))
            mv.send(outs[t].at[quarters(t, c, 0, 2)], outs[t].at[quarters(t, c, 0, 2)], (x, y, 1 - c),
                    landing=outs[t].at[quarters(t, 1 - c, 0, 2)], send_if=came, recv_if=came)

    arrays = [a for _, a in from_chip] + list(from_all)
    shapes = [jax.ShapeDtypeStruct(a.shape, a.dtype) for _, a in from_chip]
    shapes += [jax.ShapeDtypeStruct((N_CHIPS,) + a.shape, a.dtype) for a in from_all]
    n_remote = len(CHIP_FLIPS) * (n1 - len(split) + n2) + 2 * len(near) * len(split)
    relays = ((len(near) * len(split), forward), (len(split), to_sibling)) if split else None
    return _Exchange(arrays, shapes, n_remote, n1 + n2, build, relays, in_place=split)


def _scatter_exchange(to_chip, to_all, via_neighbours=False):
    n1, n2 = len(to_chip), len(to_all)
    near = CHIP_FLIPS[:2]
    direct = near if via_neighbours else CHIP_FLIPS

    def half(t, g):
        n = to_chip[t][1].shape[0] // 2
        return pl.ds(g * n, n)

    def build(mv, ins, outs):
        x, y, c = _position()
        if via_neighbours:
            for t, (chip, _) in enumerate(to_chip):
                for g, (gx, gy) in enumerate(near):
                    ox, oy = near[1 - g]
                    mv.send(ins[t].at[half(t, g)], outs[n1 + n2 + t], (_flip(x, gx), _flip(y, gy), c),
                            landing=outs[n1 + n2 + t], first=True, send_if=_is_chip(1 - x, 1 - y, chip),
                            recv_if=_is_chip(_flip(x, ox), _flip(y, oy), chip))
        for f, (fx, fy) in enumerate(CHIP_FLIPS):
            px, py = _flip(x, fx), _flip(y, fy)
            peer = (px, py, c)
            if (fx, fy) in direct:
                for t, (chip, _) in enumerate(to_chip):
                    mv.send(ins[t], outs[t].at[f], peer, landing=outs[t].at[f],
                            send_if=_is_chip(px, py, chip), recv_if=_is_chip(x, y, chip))
            for t in range(n2):
                mv.send(ins[n1 + t].at[2 * px + py], outs[n1 + t].at[f], peer, landing=outs[n1 + t].at[f])

    def forward(mv, ins, outs):
        x, y, c = _position()
        for t, (chip, _) in enumerate(to_chip):
            for g in range(len(near)):
                ox, oy = near[1 - g]
                far_slot = outs[t].at[len(near)].at[half(t, g)]
                mv.send(outs[n1 + n2 + t], far_slot, (_flip(x, ox), _flip(y, oy), c), landing=far_slot,
                        send_if=_is_chip(_flip(x, ox), _flip(y, oy), chip), recv_if=_is_chip(x, y, chip))

    arrays = [a for _, a in to_chip] + list(to_all)
    shapes = [jax.ShapeDtypeStruct((len(CHIP_FLIPS),) + a.shape, a.dtype) for _, a in to_chip]
    shapes += [jax.ShapeDtypeStruct((len(CHIP_FLIPS),) + a.shape[1:], a.dtype) for a in to_all]
    if not via_neighbours:
        return _Exchange(arrays, shapes, len(CHIP_FLIPS) * (n1 + n2), 0, build)
    shapes += [jax.ShapeDtypeStruct((a.shape[0] // 2, a.shape[1]), a.dtype) for _, a in to_chip]
    return _Exchange(arrays, shapes, 2 * len(near) * n1 + len(CHIP_FLIPS) * n2, 0, build,
                     relays=((len(near) * n1, forward), None), n_staging=n1)


def _swap_sibling(tensors, name):
    n = len(tensors)

    def body(*refs):
        ins, outs = refs[:n], refs[n:2 * n]
        send_sems, recv_sems = refs[2 * n:]
        x, y, c = _position()
        copies = [pltpu.make_async_remote_copy(
            src_ref=ins[t], dst_ref=outs[t], send_sem=send_sems.at[t], recv_sem=recv_sems.at[t],
            device_id=(x, y, 1 - c), device_id_type=MESH) for t in range(n)]
        for cp in copies:
            cp.start()
        for cp in copies:
            cp.wait_recv()
        for cp in copies:
            cp.wait_send()

    return pl.pallas_call(
        body, name=name, in_specs=[ANY] * n, out_specs=[ANY] * n,
        out_shape=[jax.ShapeDtypeStruct(a.shape, a.dtype) for a in tensors],
        scratch_shapes=[pltpu.SemaphoreType.DMA((n,)), pltpu.SemaphoreType.DMA((n,))],
        compiler_params=pltpu.CompilerParams(has_side_effects=True),
    )(*tensors)


def _pair_halves(g):
    r, cols = g.shape
    half = r // 2

    def body(g_ref, o_ref, mine, theirs, send_sem, recv_sem, local_sem):
        x, y, c = _position()
        away = pltpu.make_async_remote_copy(
            src_ref=g_ref.at[pl.ds((1 - c) * half, half)], dst_ref=theirs, send_sem=send_sem, recv_sem=recv_sem,
            device_id=(x, y, 1 - c), device_id_type=MESH)
        kept = pltpu.make_async_copy(g_ref.at[pl.ds(c * half, half)], mine, local_sem)
        away.start()
        kept.start()
        kept.wait()
        away.wait_recv()
        o_ref[...] = (mine[...] + theirs[...]).astype(BF16)
        away.wait_send()

    return pl.pallas_call(
        body, name="pair_halves", in_specs=[ANY], out_specs=pl.BlockSpec(memory_space=pltpu.VMEM),
        out_shape=jax.ShapeDtypeStruct((half, cols), BF16),
        scratch_shapes=[pltpu.VMEM((half, cols), F32), pltpu.VMEM((half, cols), F32),
                        pltpu.SemaphoreType.DMA(()), pltpu.SemaphoreType.DMA(()), pltpu.SemaphoreType.DMA(())],
        compiler_params=pltpu.CompilerParams(has_side_effects=True, vmem_limit_bytes=VMEM_LIMIT),
    )(g)


def _allreduce_small(slab):
    stages = 3

    def body(x_ref, o_ref, buf, send_sems, recv_sems):
        x, y, c = _position()
        peers = ((1 - x, y, c), (x, 1 - y, c), (x, y, 1 - c))
        o_ref[...] = x_ref[...]
        for k, peer in enumerate(peers):
            cp = pltpu.make_async_remote_copy(src_ref=o_ref, dst_ref=buf.at[k], send_sem=send_sems.at[k],
                                              recv_sem=recv_sems.at[k], device_id=peer, device_id_type=MESH)
            cp.start()
            cp.wait()
            o_ref[...] = o_ref[...] + buf[k]

    return pl.pallas_call(
        body, name="allreduce_small",
        in_specs=[pl.BlockSpec(memory_space=pltpu.VMEM)], out_specs=pl.BlockSpec(memory_space=pltpu.VMEM),
        out_shape=jax.ShapeDtypeStruct(slab.shape, slab.dtype),
        scratch_shapes=[pltpu.VMEM((stages,) + slab.shape, slab.dtype),
                        pltpu.SemaphoreType.DMA((stages,)), pltpu.SemaphoreType.DMA((stages,))],
        compiler_params=pltpu.CompilerParams(has_side_effects=True),
    )(slab)


def _row_tile(r):
    return min(r, 256)


def _sum4(stack, recv, me):
    _, r, c = stack.shape
    tr = _row_tile(r)

    def body(me_ref, own_ref, recv_ref, o_ref):
        o_ref[...] = (((own_ref[...] + recv_ref[0].astype(F32)) + recv_ref[1].astype(F32))
                      + recv_ref[2].astype(F32))

    return pl.pallas_call(
        body, name="sum_partials",
        grid_spec=pltpu.PrefetchScalarGridSpec(
            num_scalar_prefetch=1, grid=(r // tr,),
            in_specs=[pl.BlockSpec((None, tr, c), lambda i, me_ref: (me_ref[0], i, 0)),
                      pl.BlockSpec((len(CHIP_FLIPS), tr, c), lambda i, me_ref: (0, i, 0))],
            out_specs=pl.BlockSpec((tr, c), lambda i, me_ref: (i, 0))),
        out_shape=jax.ShapeDtypeStruct((r, c), F32), compiler_params=_params("parallel"),
    )(me, stack, recv)


def _sum_block(own, recv):
    r, c = recv.shape[1:]
    tr = _row_tile(r)

    def body(own_ref, recv_ref, o_ref):
        o_ref[...] = (((own_ref[:, :c] + recv_ref[0].astype(F32)) + recv_ref[1].astype(F32))
                      + recv_ref[2].astype(F32))

    return pl.pallas_call(
        body, name="sum_block", grid=(r // tr,),
        in_specs=[pl.BlockSpec((tr, own.shape[1]), lambda i: (i, 0)),
                  pl.BlockSpec((len(CHIP_FLIPS), tr, c), lambda i: (0, i, 0))],
        out_specs=pl.BlockSpec((tr, c), lambda i: (i, 0)),
        out_shape=jax.ShapeDtypeStruct((r, c), F32), compiler_params=_params("parallel"),
    )(own, recv)


def _sum_half(own, recv, core):
    r, c = own.shape[0], recv.shape[2]
    tr = _row_tile(r // 2)
    per_half = r // 2 // tr

    def body(core_ref, own_ref, recv_ref, o_ref):
        mine = pl.program_id(0) // per_half == core_ref[0]

        @pl.when(mine)
        def _():
            o_ref[...] = (((own_ref[:, :c] + recv_ref[0].astype(F32)) + recv_ref[1].astype(F32))
                          + recv_ref[2].astype(F32))

        @pl.when(jnp.logical_not(mine))
        def _():
            o_ref[...] = own_ref[:, :c]

    return pl.pallas_call(
        body, name="sum_half",
        grid_spec=pltpu.PrefetchScalarGridSpec(
            num_scalar_prefetch=1, grid=(r // tr,),
            in_specs=[pl.BlockSpec((tr, own.shape[1]), lambda i, core: (i, 0)),
                      pl.BlockSpec((len(CHIP_FLIPS), tr, c), lambda i, core: (0, i % per_half, 0))],
            out_specs=pl.BlockSpec((tr, c), lambda i, core: (i, 0))),
        out_shape=jax.ShapeDtypeStruct((r, c), F32), compiler_params=_params("parallel"),
    )(core, own, recv)


def _adamw_math(w, g, m, v):
    m = ADAM_B1 * m + (1.0 - ADAM_B1) * g
    v = ADAM_B2 * v + (1.0 - ADAM_B2) * (g * g)
    m_hat = m / (1.0 - ADAM_B1 ** ADAM_STEP)
    v_hat = v / (1.0 - ADAM_B2 ** ADAM_STEP)
    delta = -ADAM_LR * (m_hat / (jnp.sqrt(v_hat) + ADAM_EPS) + ADAM_WD * w)
    return delta, m, v


def _adamw(w, m, v, g_parts, name):
    r, c = w.shape
    tr = _row_tile(r)
    n = len(g_parts)

    def body(*refs):
        w_ref, m_ref, v_ref = refs[:3]
        g_refs = refs[3:3 + n]
        g_out, d_out, m_out, v_out, zero_out = refs[3 + n:]
        g = g_refs[0][...]
        for ref in g_refs[1:]:
            g = g + ref[...]
        g_out[...] = g
        d_out[...], m_out[...], v_out[...] = _adamw_math(w_ref[...], g, m_ref[...], v_ref[...])
        zero_out[0] = 0

    blk = pl.BlockSpec((tr, c), lambda i: (i, 0))
    return pl.pallas_call(
        body, name=name, grid=(r // tr,), in_specs=[blk] * (3 + n),
        out_specs=[blk] * 4 + [pl.BlockSpec(memory_space=pltpu.SMEM)],
        out_shape=[jax.ShapeDtypeStruct((r, c), F32)] * 4 + [jax.ShapeDtypeStruct((1,), jnp.int32)],
        compiler_params=_params("arbitrary"),
    )(w, m, v, *g_parts)


def _adamw_small(total, w, m, v):
    sizes = [w[n].size for n in SMALL]
    flat = lambda d: [d[n].reshape(1, -1) for n in SMALL]
    k = len(SMALL)

    def body(*refs):
        total_ref, w_refs, m_refs, v_refs = refs[0], refs[1:1 + k], refs[1 + k:1 + 2 * k], refs[1 + 2 * k:1 + 3 * k]
        outs = refs[1 + 3 * k:]
        for i, size in enumerate(sizes):
            g = total_ref[i:i + 1, 0:size]
            outs[i][...] = g
            outs[k + i][...], outs[2 * k + i][...], outs[3 * k + i][...] = _adamw_math(
                w_refs[i][...], g, m_refs[i][...], v_refs[i][...])

    res = pl.pallas_call(
        body, name="adamw_small", out_shape=[jax.ShapeDtypeStruct((1, size), F32) for size in sizes] * 4,
        compiler_params=_params(),
    )(total, *flat(w), *flat(m), *flat(v))
    return [{n: res[j * k + i].reshape(w[n].shape) for i, n in enumerate(SMALL)} for j in range(4)]


SHARDED = ("w_in", "w_lora_up", "a_lora_up", "w_out_a", "w_out_b", "w_out")
ROW_SHARDED = ("w_out",)
SMALL = ("norm_g", "shift_mu", "w0", "a0", "k_k", "k_a", "r_k", "lnx_w", "lnx_b", "f_bias", "q_norm_g", "k_norm_g",
         "final_norm_g")
WEIGHTS = ("norm_g", "w_in", "shift_mu", "w_lora_up", "w0", "a_lora_up", "a0", "k_k", "k_a", "r_k", "lnx_w", "lnx_b",
           "f_bias", "q_norm_g", "k_norm_g", "w_out_a", "w_out_b", "w_out", "final_norm_g")
SLAB_ROWS = 16
SLAB_COLS = SEC


def _to_slab(named, extra=None):
    rows = [jnp.pad(named[n].reshape(1, -1), ((0, 0), (0, SLAB_COLS - named[n].size))) for n in SMALL]
    if extra is not None:
        rows.append(jnp.pad(extra.reshape(1, -1), ((0, 0), (0, SLAB_COLS - extra.size))))
    rows.append(jnp.zeros((SLAB_ROWS - len(rows), SLAB_COLS), F32))
    return jnp.concatenate(rows, axis=0)


def _by_chip(g, name):
    if name in ROW_SHARDED:
        return g.reshape(N_CHIPS, g.shape[0] // N_CHIPS, g.shape[1])
    r, c = g.shape
    return g.reshape(r, N_CHIPS, c // N_CHIPS).transpose(1, 0, 2)


def _from_chips(stack, name):
    if name in ROW_SHARDED:
        return stack.reshape(-1, stack.shape[2])
    _, r, c = stack.shape
    return stack.transpose(1, 0, 2).reshape(r, N_CHIPS * c)


def kernel(x, norm_g, w_in, shift_mu, w_lora_up, w0, a_lora_up, a0, k_k, k_a, r_k, lnx_w, lnx_b, f_bias, q_norm_g, k_norm_g, w_out_a, w_out_b, w_out, final_norm_g, loss_target, m_norm_g, m_w_in, m_shift_mu, m_w_lora_up, m_w0, m_a_lora_up, m_a0, m_k_k, m_k_a, m_r_k, m_lnx_w, m_lnx_b, m_f_bias, m_q_norm_g, m_k_norm_g, m_w_out_a, m_w_out_b, m_w_out, m_final_norm_g, v_norm_g, v_w_in, v_shift_mu, v_w_lora_up, v_w0, v_a_lora_up, v_a0, v_k_k, v_k_a, v_r_k, v_lnx_w, v_lnx_b, v_f_bias, v_q_norm_g, v_k_norm_g, v_w_out_a, v_w_out_b, v_w_out, v_final_norm_g):
    w = dict(norm_g=norm_g, w_in=w_in, shift_mu=shift_mu, w_lora_up=w_lora_up, w0=w0, a_lora_up=a_lora_up, a0=a0,
             k_k=k_k, k_a=k_a, r_k=r_k, lnx_w=lnx_w, lnx_b=lnx_b, f_bias=f_bias, q_norm_g=q_norm_g,
             k_norm_g=k_norm_g, w_out_a=w_out_a, w_out_b=w_out_b, w_out=w_out, final_norm_g=final_norm_g)
    m = dict(norm_g=m_norm_g, w_in=m_w_in, shift_mu=m_shift_mu, w_lora_up=m_w_lora_up, w0=m_w0,
             a_lora_up=m_a_lora_up, a0=m_a0, k_k=m_k_k, k_a=m_k_a, r_k=m_r_k, lnx_w=m_lnx_w, lnx_b=m_lnx_b,
             f_bias=m_f_bias, q_norm_g=m_q_norm_g, k_norm_g=m_k_norm_g, w_out_a=m_w_out_a, w_out_b=m_w_out_b,
             w_out=m_w_out, final_norm_g=m_final_norm_g)
    v = dict(norm_g=v_norm_g, w_in=v_w_in, shift_mu=v_shift_mu, w_lora_up=v_w_lora_up, w0=v_w0,
             a_lora_up=v_a_lora_up, a0=v_a0, k_k=v_k_k, k_a=v_k_a, r_k=v_r_k, lnx_w=v_lnx_w, lnx_b=v_lnx_b,
             f_bias=v_f_bias, q_norm_g=v_q_norm_g, k_norm_g=v_k_norm_g, w_out_a=v_w_out_a, w_out_b=v_w_out_b,
             w_out=v_w_out, final_norm_g=v_final_norm_g)
    shapes = {n: w[n].shape for n in WEIGHTS}

    shard = {n: w[n][0].astype(BF16) for n in SHARDED}
    late = ("w_out_a", "w_out_b", "w_out")
    loras = ("w_lora_up", "a_lora_up")
    w_in_head, w_in_tail = shard["w_in"][:, :A_TAIL], shard["w_in"][:, A_TAIL:]
    shard0, shard1_head, up_stack, aup_stack = _run_on_sequencer(_gather_exchange(
        [(0, shard["w_in"]), (1, w_in_head)], [shard[n] for n in loras], split=(0, 1)), "gather_early", 2)
    moments = (_row_major_copy(m["w_in"][0], "m_w_in_rows"), _row_major_copy(v["w_in"][0], "v_w_in_rows"))
    shard0, moments = lax.optimization_barrier((shard0, moments))
    w_a = jnp.concatenate([shard0, shard1_head], axis=1)

    def late_weights(arrived):
        shard1_tail, shard2, shard3 = arrived[:3]
        w_b = jnp.concatenate([shard1_tail, shard2[:, :B_TAIL], jnp.zeros((D_MODEL, SEC - FOX_REAL), BF16)], axis=1)
        w_g = jnp.concatenate([shard2[:, B_TAIL:], shard3], axis=1)
        return (w_b, w_g, *[_from_chips(s, n) for n, s in zip(late, arrived[3:])])

    own = {}
    cut = {"block0": lambda: own["dw_a"][:, :SHARD_COLS], "head1": lambda: own["dw_a"][:, SHARD_COLS:],
           "tail1": lambda: own["dw_b"][:, :B_HEAD],
           "block2": lambda: jnp.concatenate([own["dw_b"][:, B_HEAD:FOX_REAL], own["dw_g"][:, :G_HEAD]], axis=1),
           "block3": lambda: own["dw_g"][:, G_HEAD:]}

    def bwd_exchange(dw_b, dw_g, dwa, dwb, dwo):
        own.update(dw_b=dw_b, dw_g=dw_g)
        own.update({n: _by_chip(g, n) for n, g in zip(late, (dwa, dwb, dwo))})
        return _scatter_exchange([(1, cut["tail1"]().astype(BF16)), (2, cut["block2"]().astype(BF16)),
                                  (3, cut["block3"]().astype(BF16))], [own[n].astype(BF16) for n in late])

    def tail_exchange(dw_a, dw_up, da_up):
        own.update(dw_a=dw_a)
        own.update({n: _by_chip(g, n) for n, g in zip(loras, (dw_up, da_up))})
        pair = _pair_halves(dw_a)
        return _scatter_exchange([(0, pair[:, :SHARD_COLS]), (1, pair[:, SHARD_COLS:])],
                                 [own[n].astype(BF16) for n in loras], via_neighbours=True)

    small = {n: w[n] for n in SMALL}
    loss_vec, grad_x, grads, sent, sent_last = _device_grads(
        x[0], loss_target[0], small, w_a, _from_chips(up_stack, "w_lora_up"), _from_chips(aup_stack, "a_lora_up"),
        late_weights, _gather_exchange([(1, w_in_tail), (2, shard["w_in"]), (3, shard["w_in"])], [shard[n] for n in late]),
        bwd_exchange, tail_exchange)

    total = _allreduce_small(_to_slab(grads, extra=loss_vec))
    loss = (0.5 / D_MODEL) * jnp.sum(total[len(SMALL)])
    out_g, out_d, out_m, out_v = _adamw_small(total, w, m, v)

    xpos, ypos, cpos = _position()
    me = (2 * xpos + ypos).astype(jnp.int32).reshape(1)
    core = cpos.astype(jnp.int32).reshape(1)
    core_sum, theirs = {}, {}

    def update(n):
        m_n, v_n = moments if n == "w_in" else (m[n][0], v[n][0])
        g, d, m2, v2, zero = _adamw(w[n][0], m_n, v_n, [core_sum[n], theirs[n]], "adamw_" + n)
        out_g[n], out_d[n], out_m[n], out_v[n] = (a.reshape(shapes[n]) for a in (g, d, m2, v2))
        return zero

    sent_last, out_d["norm_g"] = lax.optimization_barrier((sent_last, out_d["norm_g"]))
    core_sum["w_in"] = lax.switch(me[0], [
        lambda: _sum_half(own["dw_a"], sent_last[0], core),
        lambda: jnp.concatenate([_sum_half(cut["head1"](), sent_last[1], core),
                                 _sum_block(own["dw_b"], sent[0])], axis=1),
        lambda: _sum_block(cut["block2"](), sent[1]),
        lambda: _sum_block(cut["block3"](), sent[2])])
    core_sum.update({n: _sum4(own[n], r, me) for n, r in zip(loras, sent_last[2:])})
    rest = ("w_in",) + loras
    theirs.update(zip(rest, _swap_sibling([core_sum[n] for n in rest], "swap_sibling")))
    after_w_in = me + [update(n) for n in rest][0]
    core_sum.update({n: _sum4(own[n], r, after_w_in) for n, r in zip(late, sent[3:])})
    theirs.update(zip(late, _swap_sibling([core_sum[n] for n in late], "swap_sibling_late")))
    for n in late:
        update(n)

    return (loss, grad_x.reshape(x.shape), *[out_g[n] for n in WEIGHTS], *[out_d[n] for n in WEIGHTS],
            *[out_m[n] for n in WEIGHTS], *[out_v[n] for n in WEIGHTS])
```

```python
import functools
import math

import jax
import jax.numpy as jnp
from jax import lax
from jax.experimental import pallas as pl
from jax.experimental.pallas import tpu as pltpu
from jax.experimental.pallas import tpu_sc as plsc

F32 = jnp.float32
BF16 = jnp.bfloat16

D_MODEL = 1024
D_HALF = 512
HEAD = 64
N_HEADS = 8
LORA = 64
RWKV_COLS = 2176
FOX_REAL = 2056
SEC = 2176
GATE_COLS = 2048
IN_COLS = 6280
N_CHIPS = 4
SHARD_COLS = IN_COLS // N_CHIPS
A_TAIL = RWKV_COLS - SHARD_COLS
B_HEAD = SHARD_COLS - A_TAIL
B_TAIL = FOX_REAL - B_HEAD
G_HEAD = SHARD_COLS - B_TAIL
RMS_EPS = 1e-6
LNX_EPS = 64e-5
ATT_SCALE = HEAD ** -0.5
NEG = -1e30

ADAM_LR = 0.001
ADAM_B1 = 0.9
ADAM_B2 = 0.999
ADAM_EPS = 1e-08
ADAM_WD = 0.01
ADAM_STEP = 10

LANES = 128
SUBLANES = 8
VMEM_LIMIT = 56 * 1024 * 1024
MESH = pl.DeviceIdType.MESH


def _params(*sem):
    return pltpu.CompilerParams(dimension_semantics=sem if sem else None, vmem_limit_bytes=VMEM_LIMIT)


def _sigmoid(x):
    return 1.0 / (1.0 + jnp.exp(-x))


def _log_sigmoid(x):
    return jnp.minimum(x, 0.0) - jnp.log(1.0 + jnp.exp(-jnp.abs(x)))


def _head_ones():
    r = lax.broadcasted_iota(jnp.int32, (LANES, LANES), 0) >> 6
    c = lax.broadcasted_iota(jnp.int32, (LANES, LANES), 1) >> 6
    return (r == c).astype(BF16)


def _split3(x):
    hi = x.astype(BF16)
    r1 = x - hi.astype(F32)
    mid = r1.astype(BF16)
    lo = (r1 - mid.astype(F32)).astype(BF16)
    return hi, mid, lo


def _exact_dot(x, ones_bf16, ones_first=False):
    out = None
    for piece in _split3(x):
        if ones_first:
            t = jnp.dot(ones_bf16, piece, preferred_element_type=F32)
        else:
            t = jnp.dot(piece, ones_bf16, preferred_element_type=F32)
        out = t if out is None else out + t
    return out


def _head_sum(x, bd):
    n = x.shape[1] // LANES
    parts = [_exact_dot(x[:, i * LANES:(i + 1) * LANES], bd) for i in range(n)]
    return parts[0] if n == 1 else jnp.concatenate(parts, axis=1)


def _dot_nt(a, b):
    return lax.dot_general(a, b, (((1,), (1,)), ((), ())), preferred_element_type=F32)


def _dot_tn(a, b):
    return lax.dot_general(a, b, (((0,), (0,)), ((), ())), preferred_element_type=F32)


def _colsum(x):
    return jnp.sum(x, axis=0, keepdims=True)


def _matmul_tn_acc(at, b, name, tk=512):
    m, k = at.shape
    n = b.shape[1]

    def body(a_ref, b_ref, o_ref):
        j = pl.program_id(0)

        @pl.when(j == 0)
        def _():
            o_ref[...] = jnp.zeros_like(o_ref)

        o_ref[...] += jnp.dot(a_ref[...], b_ref[...].astype(BF16), preferred_element_type=F32)

    return pl.pallas_call(
        body, name=name, grid=(k // tk,),
        in_specs=[pl.BlockSpec((m, tk), lambda j: (0, j)), pl.BlockSpec((tk, n), lambda j: (j, 0))],
        out_specs=pl.BlockSpec((m, n), lambda j: (0, 0)),
        out_shape=jax.ShapeDtypeStruct((m, n), F32), compiler_params=_params("arbitrary"),
    )(at, b)


def _inproj_bwd(du_a, du_b, du_g, w_a, w_b, w_g, x, dx2, g, exchange=None, tm=256):
    s, d = x.shape
    nb = s // tm

    def body(*refs):
        ((da_ref, db_ref, dg_ref, wa_ref, wb_ref, wg_ref, x_ref, dx2_ref, g_ref), (gx_ref, gg_ref), _,
         moves) = _split_refs(refs, 9, 2, exchange)
        i = pl.program_id(0)
        if moves:
            moves.start(also=(i == 0))

        @pl.when(i == 0)
        def _():
            gg_ref[...] = jnp.zeros_like(gg_ref)

        dh = _dot_nt(da_ref[...].astype(BF16), wa_ref[...])
        dh += _dot_nt(db_ref[...].astype(BF16), wb_ref[...])
        dh += _dot_nt(dg_ref[...].astype(BF16), wg_ref[...])
        xv = x_ref[...]
        r = lax.rsqrt(jnp.mean(xv * xv, axis=-1, keepdims=True) + RMS_EPS)
        xh = xv * r
        gg_ref[...] += _colsum(dh * xh)
        dxh = dh * g_ref[...]
        gx_ref[...] = dx2_ref[...] + r * (dxh - xh * jnp.mean(dxh * xh, axis=-1, keepdims=True))
        if moves:
            moves.wait(also=(i == nb - 1))

    row = lambda w: pl.BlockSpec((tm, w), lambda i: (i, 0))
    full = lambda a: pl.BlockSpec(a.shape, lambda i: (0, 0))
    ex_in = exchange.operands if exchange else []
    ex_out = exchange.out_shapes if exchange else []
    res = pl.pallas_call(
        body, name="inproj_bwd", grid=(nb,),
        in_specs=[row(SEC), row(SEC), row(GATE_COLS), full(w_a), full(w_b), full(w_g), row(d), row(d), full(g)]
                 + [ANY] * len(ex_in),
        out_specs=[row(d), pl.BlockSpec((1, d), lambda i: (0, 0))] + [ANY] * len(ex_out),
        out_shape=[jax.ShapeDtypeStruct((s, d), F32), jax.ShapeDtypeStruct((1, d), F32)] + ex_out,
        scratch_shapes=exchange.scratch() if exchange else [],
        compiler_params=_params("arbitrary"),
    )(du_a, du_b, du_g, w_a, w_b, w_g, x, dx2, g, *ex_in)
    return res[0], res[1], list(res[2:])


def _rwkv_elementwise(ua, prev_row, first, mu, wl, w0, a0, kkw, kaw, bd):
    tm = ua.shape[0]
    rows = lax.broadcasted_iota(jnp.int32, (tm, 1), 0)
    prev = jnp.where(first, jnp.zeros_like(prev_row), prev_row)
    shifted = jnp.where(rows == 0, prev, pltpu.roll(ua, 1, 0))
    delta = shifted - ua
    us = ua + delta * mu
    r = us[:, 0:512]
    k0 = us[:, 512:1024]
    v = us[:, 1024:1536]
    lo = us[:, 1536:1664]
    gate = us[:, 1664:2176]
    lane = lax.broadcasted_iota(jnp.int32, (1, LANES), 1)
    th = jnp.tanh(lo)
    lin = jnp.where(lane < LORA, th, lo)
    ll = jnp.dot(lin.astype(BF16), wl, preferred_element_type=F32)
    sz = _sigmoid(w0 + ll[:, :512])
    e = sz * math.exp(-0.5)
    dec = jnp.exp(-e)
    a = _sigmoid(a0 + ll[:, 512:])
    kk0 = k0 * kkw
    ss = _head_sum(kk0 * kk0, bd)
    nrm = jnp.maximum(jnp.sqrt(ss), 1e-12)
    kk = kk0 / nrm
    k = k0 * (1.0 + (a - 1.0) * kaw)
    return dict(delta=delta, us=us, r=r, k0=k0, v=v, lo=lo, gate=gate, th=th, lin=lin, sz=sz, e=e, dec=dec,
                a=a, kk0=kk0, ss=ss, nrm=nrm, kk=kk, k=k)


def _rwkv_front(x, g, w_a, mu, wl, w0, a0, kkw, kaw, tm=256):
    s, d = x.shape

    def body(x_ref, g_ref, wa_ref, mu_ref, wl_ref, w0_ref, a0_ref, kkw_ref, kaw_ref,
             h_ref, ua_ref, r_ref, w_ref, k_ref, v_ref, a_ref, b_ref, gate_ref, last_row):
        i = pl.program_id(0)
        xv = x_ref[...]
        h = (xv * lax.rsqrt(jnp.mean(xv * xv, axis=-1, keepdims=True) + RMS_EPS) * g_ref[...]).astype(BF16)
        h_ref[...] = h
        ua = jnp.dot(h, wa_ref[...], preferred_element_type=F32)
        ua_ref[...] = ua

        @pl.when(i == 0)
        def _():
            last_row[...] = jnp.zeros_like(last_row)

        f = _rwkv_elementwise(ua, last_row[...], i == 0, mu_ref[...], wl_ref[...], w0_ref[...],
                              a0_ref[...], kkw_ref[...], kaw_ref[...], _head_ones())
        last_row[...] = ua[tm - 1:tm, :]
        r_ref[...] = f["r"]
        w_ref[...] = f["dec"]
        k_ref[...] = f["k"]
        v_ref[...] = f["v"]
        a_ref[...] = -f["kk"]
        b_ref[...] = f["kk"] * f["a"]
        gate_ref[...] = f["gate"]

    vec = lambda w: pl.BlockSpec((1, w), lambda i: (0, 0))
    row = lambda w: pl.BlockSpec((tm, w), lambda i: (i, 0))
    return pl.pallas_call(
        body, name="rwkv_front", grid=(s // tm,),
        in_specs=[row(d), vec(d), pl.BlockSpec(w_a.shape, lambda i: (0, 0), pipeline_mode=pl.Buffered(1)),
                  vec(SEC), pl.BlockSpec((LANES, 2 * D_HALF), lambda i: (0, 0)),
                  vec(D_HALF), vec(D_HALF), vec(D_HALF), vec(D_HALF)],
        out_specs=[row(d), row(SEC)] + [row(D_HALF)] * 7,
        out_shape=[jax.ShapeDtypeStruct((s, d), BF16), jax.ShapeDtypeStruct((s, SEC), F32)]
                  + [jax.ShapeDtypeStruct((s, D_HALF), F32)] * 7,
        scratch_shapes=[pltpu.VMEM((1, SEC), F32)],
        compiler_params=_params("arbitrary"),
    )(x, g, w_a, mu, wl, w0, a0, kkw, kaw)


SCAN_TB = 128
N_PAIRS = 4


def _pair_sum(x, left):
    s_l = jnp.sum(jnp.where(left, x, 0.0), axis=1, keepdims=True)
    s_r = jnp.sum(jnp.where(left, 0.0, x), axis=1, keepdims=True)
    return jnp.where(left, s_l, s_r)


def _pair_dot(x, row_l, row_r, left):
    s_l = jnp.sum(x * row_l, axis=1, keepdims=True)
    s_r = jnp.sum(x * row_r, axis=1, keepdims=True)
    return jnp.where(left, s_l, s_r)


def _halves(rows8):
    lane = lax.broadcasted_iota(jnp.int32, rows8.shape, 1)
    keep_left = (lane & (LANES - 1)) < HEAD
    return jnp.where(keep_left, rows8, 0.0), jnp.where(keep_left, 0.0, rows8)


def _quad_consts():
    lane = lax.broadcasted_iota(jnp.int32, (HEAD, 2 * LANES), 1)
    rowi = lax.broadcasted_iota(jnp.int32, (HEAD, 2 * LANES), 0)
    diag2 = rowi == (lane & (HEAD - 1))
    r = lax.broadcasted_iota(jnp.int32, (2 * LANES, 2 * LANES), 0) >> 6
    c = lax.broadcasted_iota(jnp.int32, (2 * LANES, 2 * LANES), 1) >> 6
    return diag2, (r == c).astype(BF16)


def _rows_to_columns(x8, diag2, bd2):
    lhs = jnp.concatenate([jnp.where(diag2, x8[i:i + 1], 0.0).astype(BF16) for i in range(SUBLANES)], axis=0)
    return jnp.dot(lhs, bd2, preferred_element_type=F32)


def _diag_rows(qtile, diag2, bd2, sub_row2):
    res = jnp.dot(qtile, bd2, preferred_element_type=F32)
    out = jnp.zeros((SUBLANES, 2 * LANES), F32)
    for i in range(SUBLANES):
        out = jnp.where(sub_row2 == i, _colsum(jnp.where(diag2, res[i * HEAD:(i + 1) * HEAD], 0.0)), out)
    return out


def _store_tile(qbuf, slot, p, i, x):
    qbuf[slot, p // 2, i * HEAD:(i + 1) * HEAD, (p % 2) * LANES:(p % 2 + 1) * LANES] = x.astype(BF16)


def _left_half():
    return lax.broadcasted_iota(jnp.int32, (HEAD, LANES), 1) < HEAD


def _split_refs(refs, n_rows, n_out, exchange):
    n_in = len(exchange.operands) if exchange else 0
    n_ex_out = len(exchange.out_shapes) if exchange else 0
    refs = list(refs)
    rows, refs = refs[:n_rows], refs[n_rows:]
    ex_in, refs = refs[:n_in], refs[n_in:]
    outs, refs = refs[:n_out], refs[n_out:]
    ex_out, refs = refs[:n_ex_out], refs[n_ex_out:]
    scratch, sems = (refs[:-3], refs[-3:]) if exchange else (refs, None)
    moves = exchange.moves(ex_in, ex_out, sems) if exchange else None
    return rows, outs, scratch, moves


def _wkv_fwd(r, w, k, a, b, v, exchange=None):
    s = r.shape[0]
    tb = SCAN_TB
    nb = s // tb

    def body(*refs):
        (r_ref, w_ref, k_ref, a_ref, b_ref, v_ref), (y_ref, st_ref), (state, vbuf, qbuf), moves = _split_refs(
            refs, 6, 2, exchange)
        g = pl.program_id(0)
        if moves:
            moves.start(also=(g == 0))

        @pl.when(g == 0)
        def _():
            state[...] = jnp.zeros_like(state)
            qbuf[...] = jnp.zeros_like(qbuf)

        left = _left_half()
        diag2, bd2 = _quad_consts()
        sub_row2 = lax.broadcasted_iota(jnp.int32, (SUBLANES, 2 * LANES), 0)
        groups = tb // SUBLANES
        quads = [slice(g2 * 2 * LANES, (g2 + 1) * 2 * LANES) for g2 in range(2)]

        def rows_of(q):
            return pl.ds(pl.multiple_of(q * SUBLANES, SUBLANES), SUBLANES)

        def v_tiles(q, slot):
            v8 = v_ref[rows_of(q), :]
            for g2 in range(2):
                vbuf[slot, g2] = _rows_to_columns(v8[:, quads[g2]], diag2, bd2)

        def chain(q, slot):
            rows8 = rows_of(q)
            a8, w8, b8, k8, r8 = (x[rows8, :] for x in (a_ref, w_ref, b_ref, k_ref, r_ref))
            pairs = [slice(p * LANES, (p + 1) * LANES) for p in range(N_PAIRS)]
            a_next = pltpu.roll(a8, SUBLANES - 1, 0)
            (a8_l, a8_r), (wa8_l, wa8_r) = _halves(a8), _halves(w8 * a_next)
            ba8 =jnp.concatenate([_pair_sum(b8[:, pr] * a_next[:, pr], left[0:SUBLANES]) for pr in pairs], axis=1)
            ka8 = jnp.concatenate([_pair_sum(k8[:, pr] * a_next[:, pr], left[0:SUBLANES]) for pr in pairs], axis=1)
            sp = [state[p] for p in range(N_PAIRS)]
            for i in range(0, SUBLANES, 2):
                r0, r1 = slice(i, i + 1), slice(i + 1, i + 2)
                sums = [(_pair_dot(sp[p], a8_l[r0, pairs[p]], a8_r[r0, pairs[p]], left),
                         _pair_dot(sp[p], wa8_l[r0, pairs[p]], wa8_r[r0, pairs[p]], left)) for p in range(N_PAIRS)]
                sa0, sa1 = [s[0] for s in sums], [s[1] for s in sums]
                for p in range(N_PAIRS):
                    pr = pairs[p]
                    inner = slice((p % 2) * LANES, (p % 2 + 1) * LANES)
                    vt0 = vbuf[slot, p // 2, i * HEAD:(i + 1) * HEAD, inner]
                    vt1 = vbuf[slot, p // 2, (i + 1) * HEAD:(i + 2) * HEAD, inner]
                    sa_next = sa1[p] + sa0[p] * ba8[r0, pr] + vt0 * ka8[r0, pr]
                    s1 = sp[p] * w8[r0, pr] + sa0[p] * b8[r0, pr] + vt0 * k8[r0, pr]
                    st_ref[q * SUBLANES + i, p] = s1
                    _store_tile(qbuf, slot, p, i, s1 * r8[r0, pr])
                    s2 = s1 * w8[r1, pr] + sa_next * b8[r1, pr] + vt1 * k8[r1, pr]
                    st_ref[q * SUBLANES + i + 1, p] = s2
                    _store_tile(qbuf, slot, p, i + 1, s2 * r8[r1, pr])
                    sp[p] = s2
            for p in range(N_PAIRS):
                state[p] = sp[p]

        def y_rows(q, slot):
            for g2 in range(2):
                y_ref[rows_of(q), quads[g2]] = _diag_rows(qbuf[slot, g2], diag2, bd2, sub_row2)

        v_tiles(0, 0)

        def two_groups(j, carry):
            q0 = 2 * j
            v_tiles(q0 + 1, 1)
            chain(q0, 0)
            y_rows(jnp.maximum(q0 - 1, 0), 1)
            v_tiles(jnp.minimum(q0 + 2, groups - 1), 0)
            chain(q0 + 1, 1)
            y_rows(q0, 0)
            return carry

        lax.fori_loop(0, groups // 2, two_groups, 0)
        y_rows(groups - 1, 1)
        if moves:
            moves.wait(also=(g == nb - 1))

    rows = pl.BlockSpec((tb, D_HALF), lambda g: (g, 0))
    ex_in = exchange.operands if exchange else []
    ex_out = exchange.out_shapes if exchange else []
    res = pl.pallas_call(
        body, name="wkv_fwd", grid=(nb,),
        in_specs=[rows] * 6 + [ANY] * len(ex_in),
        out_specs=[rows, pl.BlockSpec((tb, N_PAIRS, HEAD, LANES), lambda g: (g, 0, 0, 0))] + [ANY] * len(ex_out),
        out_shape=[jax.ShapeDtypeStruct((s, D_HALF), F32),
                   jax.ShapeDtypeStruct((s, N_PAIRS, HEAD, LANES), F32)] + ex_out,
        scratch_shapes=[pltpu.VMEM((N_PAIRS, HEAD, LANES), F32),
                        pltpu.VMEM((2, 2, SUBLANES * HEAD, 2 * LANES), F32),
                        pltpu.VMEM((2, 2, SUBLANES * HEAD, 2 * LANES), BF16)]
                       + (exchange.scratch() if exchange else []),
        compiler_params=_params("arbitrary"),
    )(r, w, k, a, b, v, *ex_in)
    return res[0], res[1], list(res[2:])


def _wkv_bwd(r, w, k, a, b, v, dy, st, exchange=None):
    s = r.shape[0]
    tb = SCAN_TB
    nb = s // tb

    def body(*refs):
        ((r_ref, w_ref, k_ref, a_ref, b_ref, v_ref, dy_ref, st_ref, before_ref),
         (dr_ref, dw_ref, dk_ref, dv_ref, da_ref, db_ref), (dstate, vbuf, qbuf, sbuf),
         moves) = _split_refs(refs, 9, 6, exchange)
        g = pl.program_id(0)
        first_block = g == nb - 1
        if moves:
            moves.start(also=(g == 0))

        @pl.when(g == 0)
        def _():
            dstate[...] = jnp.zeros_like(dstate)
            qbuf[...] = jnp.zeros_like(qbuf)

        left = _left_half()
        diag2, bd2 = _quad_consts()
        sub_row = lax.broadcasted_iota(jnp.int32, (SUBLANES, LANES), 0)
        sub_row2 = lax.broadcasted_iota(jnp.int32, (SUBLANES, 2 * LANES), 0)
        groups = tb // SUBLANES
        quads = [slice(g2 * 2 * LANES, (g2 + 1) * 2 * LANES) for g2 in range(2)]
        row_refs = (dr_ref, dw_ref, dk_ref, da_ref, db_ref)

        def rows_of(q):
            return pl.ds(pl.multiple_of(q * SUBLANES, SUBLANES), SUBLANES)

        def state_before(q, i, p):
            if i > 0:
                return st_ref[q * SUBLANES + i - 1, p]
            return jnp.where(q == 0, jnp.where(first_block, 0.0, before_ref[0, p]),
                             st_ref[jnp.maximum(q * SUBLANES - 1, 0), p])

        def column_tiles(q, slot):
            rows8 = rows_of(q)
            for kind, ref in enumerate((v_ref, dy_ref)):
                x8 = ref[rows8, :]
                for g2 in range(2):
                    vbuf[slot, kind, g2] = _rows_to_columns(x8[:, quads[g2]], diag2, bd2)
            a8 = a_ref[rows8, :]
            for i in range(SUBLANES):
                for p in range(N_PAIRS):
                    _store_tile(sbuf, 0, p, i, state_before(q, i, p) * a8[i:i + 1, p * LANES:(p + 1) * LANES])
            for g2 in range(2):
                vbuf[slot, 2, g2] = jnp.dot(sbuf[0, g2], bd2, preferred_element_type=F32)

        def chain(q, slot):
            rows8 = rows_of(q)
            a8, w8, b8, k8, r8 = (x[rows8, :] for x in (a_ref, w_ref, b_ref, k_ref, r_ref))
            b8_l, b8_r = _halves(b8)
            dsp = [dstate[p] for p in range(N_PAIRS)]
            outs = [[jnp.zeros((SUBLANES, LANES), F32) for _ in row_refs] for _ in range(N_PAIRS)]
            after = [st_ref[q * SUBLANES + SUBLANES - 1, p] for p in range(N_PAIRS)]
            for i in reversed(range(SUBLANES)):
                row = slice(i, i + 1)
                pl_ = [slice(p * LANES, (p + 1) * LANES) for p in range(N_PAIRS)]
                tile = [(p // 2, slice(i * HEAD, (i + 1) * HEAD), slice((p % 2) * LANES, (p % 2 + 1) * LANES))
                        for p in range(N_PAIRS)]
                sp = [state_before(q, i, p) for p in range(N_PAIRS)]
                dyt = [vbuf[(slot, 1) + tile[p]] for p in range(N_PAIRS)]
                ds = [dsp[p] + dyt[p] * r8[row, pl_[p]] for p in range(N_PAIRS)]
                dsa = [_pair_dot(ds[p], b8_l[row, pl_[p]], b8_r[row, pl_[p]], left) for p in range(N_PAIRS)]
                sa = [vbuf[(slot, 2) + tile[p]] for p in range(N_PAIRS)]
                for p in range(N_PAIRS):
                    ar, wr, br, kr = (x[row, pl_[p]] for x in (a8, w8, b8, k8))
                    vt = vbuf[(slot, 0) + tile[p]]
                    dsp[p] = ds[p] * wr + dsa[p] * ar
                    new = (_colsum(after[p] * dyt[p]), _colsum(ds[p] * sp[p]), _colsum(ds[p] * vt),
                           _colsum(sp[p] * dsa[p]), _colsum(ds[p] * sa[p]))
                    outs[p] = [jnp.where(sub_row == i, n, o) for n, o in zip(new, outs[p])]
                    _store_tile(qbuf, slot, p, i, ds[p] * kr)
                after = sp
            for p in range(N_PAIRS):
                dstate[p] = dsp[p]
                for ref, o in zip(row_refs, outs[p]):
                    ref[rows8, p * LANES:(p + 1) * LANES] = o

        def dv_rows(q, slot):
            for g2 in range(2):
                dv_ref[rows_of(q), quads[g2]] = _diag_rows(qbuf[slot, g2], diag2, bd2, sub_row2)

        column_tiles(groups - 1, 0)

        def two_groups(j, carry):
            q0 = groups - 1 - 2 * j
            column_tiles(q0 - 1, 1)
            chain(q0, 0)
            dv_rows(jnp.minimum(q0 + 1, groups - 1), 1)
            column_tiles(jnp.maximum(q0 - 2, 0), 0)
            chain(q0 - 1, 1)
            dv_rows(q0, 0)
            return carry

        lax.fori_loop(0, groups // 2, two_groups, 0)
        dv_rows(0, 1)
        if moves:
            moves.wait(also=(g == nb - 1))

    rows = pl.BlockSpec((tb, D_HALF), lambda g: (nb - 1 - g, 0))
    ex_in = exchange.operands if exchange else []
    ex_out = exchange.out_shapes if exchange else []
    res = pl.pallas_call(
        body, name="wkv_bwd", grid=(nb,),
        in_specs=[rows] * 7 + [pl.BlockSpec((tb, N_PAIRS, HEAD, LANES), lambda g: (nb - 1 - g, 0, 0, 0)),
                               pl.BlockSpec((1, N_PAIRS, HEAD, LANES),
                                            lambda g: (jnp.maximum((nb - 1 - g) * tb - 1, 0), 0, 0, 0))]
                 + [ANY] * len(ex_in),
        out_specs=[rows] * 6 + [ANY] * len(ex_out),
        out_shape=[jax.ShapeDtypeStruct((s, D_HALF), F32)] * 6 + ex_out,
        scratch_shapes=[pltpu.VMEM((N_PAIRS, HEAD, LANES), F32),
                        pltpu.VMEM((2, 3, 2, SUBLANES * HEAD, 2 * LANES), F32),
                        pltpu.VMEM((2, 2, SUBLANES * HEAD, 2 * LANES), BF16),
                        pltpu.VMEM((1, 2, SUBLANES * HEAD, 2 * LANES), BF16)]
                       + (exchange.scratch() if exchange else []),
        compiler_params=_params("arbitrary"),
    )(r, w, k, a, b, v, dy, st, st, *ex_in)
    return list(res[:6]), list(res[6:])


def _rwkv_post_math(y, r, k, v, gate, lw, lb, rk, bd):
    mean = _head_sum(y, bd) * (1.0 / HEAD)
    yc = y - mean
    var = _head_sum(yc * yc, bd) * (1.0 / HEAD)
    rstd = lax.rsqrt(var + LNX_EPS)
    yn = yc * rstd
    rkk = _head_sum(r * k * rk, bd)
    sg = _sigmoid(gate)
    pre = yn * lw + lb + rkk * v
    return yn, rstd, rkk, sg, pre


def _rwkv_prep_bwd(u_a, h_t, grads, mu, wl, w0, a0, kkw, kaw, tm=256):
    s = u_a.shape[0]
    nb = s // tm
    d = h_t.shape[0]

    def body(ua_ref, prev_ref, ht_ref, drs_ref, dws_ref, dks_ref, dvs_ref, das_ref, dbs_ref, drb_ref, dkb_ref, dvb_ref,
             dgt_ref, mu_ref, wl_ref, w0_ref, a0_ref, kkw_ref, kaw_ref,
             du_ref, dwa_ref, dmu_ref, dwl_ref, dw0_ref, da0_ref, dkkw_ref, dkaw_ref, carry):
        i = pl.program_id(0)

        @pl.when(i == 0)
        def _():
            carry[...] = jnp.zeros_like(carry)
            for ref in (dwa_ref, dmu_ref, dwl_ref, dw0_ref, da0_ref, dkkw_ref, dkaw_ref):
                ref[...] = jnp.zeros_like(ref)

        bd = _head_ones()
        mu_v, wl_v, kkw_v, kaw_v = mu_ref[...], wl_ref[...], kkw_ref[...], kaw_ref[...]
        f = _rwkv_elementwise(ua_ref[...], prev_ref[7:8, :], i == nb - 1, mu_v, wl_v, w0_ref[...],
                              a0_ref[...], kkw_v, kaw_v, bd)
        a, kk, k0 = f["a"], f["kk"], f["k0"]
        dk = dks_ref[...] + dkb_ref[...]
        dbs = dbs_ref[...]
        dkk = dbs * a - das_ref[...]
        da = dbs * kk + dk * k0 * kaw_v
        dk0 = dk * (1.0 + (a - 1.0) * kaw_v)
        dkaw_ref[...] += _colsum(dk * k0 * (a - 1.0))
        inv = 1.0 / f["nrm"]
        proj = _head_sum(dkk * kk, bd)
        dkk0 = jnp.where(f["ss"] > 1e-24, (dkk - kk * proj) * inv, dkk * inv)
        dk0 = dk0 + dkk0 * kkw_v
        dkkw_ref[...] += _colsum(dkk0 * k0)
        dza = da * a * (1.0 - a)
        da0_ref[...] += _colsum(dza)
        dz = -dws_ref[...] * f["dec"] * f["e"] * (1.0 - f["sz"])
        dw0_ref[...] += _colsum(dz)
        dll = jnp.concatenate([dz, dza], axis=1).astype(BF16)
        dwl_ref[...] += _dot_tn(f["lin"].astype(BF16), dll)
        dlin = _dot_nt(dll, wl_v)
        lane = lax.broadcasted_iota(jnp.int32, (1, LANES), 1)
        th = f["th"]
        dlo = jnp.where(lane < LORA, dlin * (1.0 - th * th), dlin)
        dus = jnp.concatenate([drs_ref[...] + drb_ref[...], dk0, dvs_ref[...] + dvb_ref[...], dlo, dgt_ref[...]],
                              axis=1)
        dmu_ref[...] += _colsum(dus * f["delta"])
        g1 = dus * mu_v
        rows = lax.broadcasted_iota(jnp.int32, (tm, 1), 0)
        up = jnp.where(rows == tm - 1, carry[...], pltpu.roll(g1, tm - 1, 0))
        dua = dus - g1 + up
        du_ref[...] = dua
        dwa_ref[...] += jnp.dot(ht_ref[...], dua.astype(BF16), preferred_element_type=F32)
        carry[...] = g1[0:1, :]

    rev = lambda w: pl.BlockSpec((tm, w), lambda i: (nb - 1 - i, 0))
    vec = lambda w: pl.BlockSpec((1, w), lambda i: (0, 0))
    wl_spec = pl.BlockSpec((LANES, 2 * D_HALF), lambda i: (0, 0))
    return pl.pallas_call(
        body, name="rwkv_prep_bwd", grid=(nb,),
        in_specs=[rev(SEC), pl.BlockSpec((8, SEC), lambda i: (jnp.maximum((nb - 1 - i) * (tm // 8) - 1, 0), 0)),
                  pl.BlockSpec((d, tm), lambda i: (0, nb - 1 - i))]
                 + [rev(D_HALF)] * 10 + [vec(SEC), wl_spec] + [vec(D_HALF)] * 4,
        out_specs=[rev(SEC), pl.BlockSpec((d, SEC), lambda i: (0, 0)), vec(SEC), wl_spec] + [vec(D_HALF)] * 4,
        out_shape=[jax.ShapeDtypeStruct((s, SEC), F32), jax.ShapeDtypeStruct((d, SEC), F32),
                   jax.ShapeDtypeStruct((1, SEC), F32),
                   jax.ShapeDtypeStruct((LANES, 2 * D_HALF), F32)] + [jax.ShapeDtypeStruct((1, D_HALF), F32)] * 4,
        scratch_shapes=[pltpu.VMEM((1, SEC), F32)],
        compiler_params=_params("arbitrary"),
    )(u_a, u_a, h_t, *grads, mu, wl, w0, a0, kkw, kaw)


def _tri(tm, lower):
    r = lax.broadcasted_iota(jnp.int32, (tm, tm), 0)
    c = lax.broadcasted_iota(jnp.int32, (tm, tm), 1)
    return ((r >= c) if lower else (r <= c)).astype(BF16)


def _head_rms(x, g, bd):
    rinv = lax.rsqrt(_head_sum(x * x, bd) * (1.0 / HEAD) + RMS_EPS)
    xh = x * rinv
    return xh, rinv, xh * g


def _fox_front(h, w_b, fb, qg, kg, tm=256):
    s, d = h.shape

    def body(h_ref, wb_ref, fb_ref, qg_ref, kg_ref, ub_ref, q_ref, k_ref, v_ref, cc_ref, cr_ref, carry):
        i = pl.program_id(0)

        @pl.when(i == 0)
        def _():
            carry[...] = jnp.zeros_like(carry)

        ub_ref[...] = jnp.dot(h_ref[...], wb_ref[...], preferred_element_type=F32)
        bd = _head_ones()
        _, _, qn = _head_rms(ub_ref[:, 0:512], qg_ref[...], bd)
        _, _, kn = _head_rms(ub_ref[:, 512:1024], kg_ref[...], bd)
        q_ref[...] = (qn * ATT_SCALE).astype(BF16)
        k_ref[...] = kn.astype(BF16)
        v_ref[...] = ub_ref[:, 1024:1536].astype(BF16)
        lane = lax.broadcasted_iota(jnp.int32, (1, LANES), 1)
        logf = jnp.where(lane < N_HEADS, _log_sigmoid(ub_ref[:, 2048:2176] + fb_ref[...]), 0.0)
        cum = _exact_dot(logf, _tri(tm, True), ones_first=True) + carry[...]
        for h in range(N_HEADS):
            cc_ref[h] = jnp.broadcast_to(cum[:, h:h + 1], (tm, LANES))
        cr_ref[...] = jnp.transpose(cum)[0:N_HEADS, :]
        carry[...] = cum[tm - 1:tm, :]

    blk = pl.BlockSpec((tm, D_HALF), lambda i: (i, 0))
    return pl.pallas_call(
        body, name="fox_front", grid=(s // tm,),
        in_specs=[pl.BlockSpec((tm, d), lambda i: (i, 0)),
                  pl.BlockSpec(w_b.shape, lambda i: (0, 0), pipeline_mode=pl.Buffered(1)),
                  pl.BlockSpec((1, LANES), lambda i: (0, 0)),
                  pl.BlockSpec((1, D_HALF), lambda i: (0, 0)), pl.BlockSpec((1, D_HALF), lambda i: (0, 0))],
        out_specs=[pl.BlockSpec((tm, SEC), lambda i: (i, 0)), blk, blk, blk,
                   pl.BlockSpec((N_HEADS, tm, LANES), lambda i: (0, i, 0)), pl.BlockSpec((N_HEADS, tm), lambda i: (0, i))],
        out_shape=[jax.ShapeDtypeStruct((s, SEC), F32)] + [jax.ShapeDtypeStruct((s, D_HALF), BF16)] * 3
                  + [jax.ShapeDtypeStruct((N_HEADS, s, LANES), F32), jax.ShapeDtypeStruct((N_HEADS, s), F32)],
        scratch_shapes=[pltpu.VMEM((1, LANES), F32)],
        compiler_params=_params("arbitrary"),
    )(h, w_b, fb, qg, kg)


ATT_T = 256


def _tiles(nblk, by_query):
    if by_query:
        pairs = [(i, j) for i in range(nblk) for j in range(i + 1)]
    else:
        pairs = [(i, j) for j in range(nblk) for i in range(j, nblk)]
    return (jnp.asarray([p[0] for p in pairs], jnp.int32), jnp.asarray([p[1] for p in pairs], jnp.int32))


def _attn_fwd(q, k, v, cc, cr):
    s = q.shape[0]
    t = ATT_T
    nblk = s // t

    def body(qi_ref, kj_ref, q_ref, k_ref, v_ref, cc_ref, cr_ref, o_ref, lse_ref, m_sc, l_sc, acc_sc):
        i = qi_ref[pl.program_id(0)]
        j = kj_ref[pl.program_id(0)]

        @pl.when(j == 0)
        def _():
            m_sc[...] = jnp.full_like(m_sc, NEG)
            l_sc[...] = jnp.zeros_like(l_sc)
            acc_sc[...] = jnp.zeros_like(acc_sc)

        def tile(on_diagonal):
            causal = _causal_tile(t) if on_diagonal else None
            left = lax.broadcasted_iota(jnp.int32, (1, LANES), 1) < HEAD
            for p in range(N_PAIRS):
                lanes = slice(p * LANES, (p + 1) * LANES)
                q2, k2, v2 = q_ref[:, lanes], k_ref[:, lanes], v_ref[:, lanes]
                acc2 = acc_sc[:, lanes]
                for e in range(2):
                    h = 2 * p + e
                    msk = left if e == 0 else jnp.logical_not(left)
                    sc = _dot_nt(jnp.where(msk, q2, jnp.zeros_like(q2)), k2)
                    sc = sc + (_wide(cc_ref[h]) - cr_ref[h:h + 1, :])
                    if on_diagonal:
                        sc = jnp.where(causal, sc, NEG)
                    m_prev = m_sc[h]
                    m_new = jnp.maximum(m_prev, jnp.max(sc, axis=1, keepdims=True))
                    alpha = jnp.exp(m_prev - m_new)
                    pm = jnp.exp(sc - _wide(m_new))
                    l_sc[h] = alpha * l_sc[h] + jnp.sum(pm, axis=1, keepdims=True)
                    m_sc[h] = m_new
                    pv = jnp.dot(pm.astype(BF16), v2, preferred_element_type=F32)
                    acc2 = jnp.where(msk, alpha * acc2 + pv, acc2)
                acc_sc[:, lanes] = acc2

        pl.when(j < i)(functools.partial(tile, False))
        pl.when(j == i)(functools.partial(tile, True))

        @pl.when(j == i)
        def _():
            left = lax.broadcasted_iota(jnp.int32, (1, LANES), 1) < HEAD
            for p in range(N_PAIRS):
                lanes = slice(p * LANES, (p + 1) * LANES)
                inv = jnp.where(left, 1.0 / l_sc[2 * p], 1.0 / l_sc[2 * p + 1])
                o_ref[:, lanes] = acc_sc[:, lanes] * inv
            for h in range(N_HEADS):
                lse_ref[h] = m_sc[h] + jnp.log(l_sc[h])

    qi, kj = _tiles(nblk, by_query=True)
    qblk = pl.BlockSpec((t, D_HALF), lambda n, qi, kj: (qi[n], 0))
    kblk = pl.BlockSpec((t, D_HALF), lambda n, qi, kj: (kj[n], 0))
    qrep = pl.BlockSpec((N_HEADS, t, LANES), lambda n, qi, kj: (0, qi[n], 0))
    return pl.pallas_call(
        body, name="fox_attn_fwd",
        grid_spec=pltpu.PrefetchScalarGridSpec(
            num_scalar_prefetch=2, grid=(qi.shape[0],),
            in_specs=[qblk, kblk, kblk, qrep, pl.BlockSpec((N_HEADS, t), lambda n, qi, kj: (0, kj[n]))],
            out_specs=[qblk, qrep],
            scratch_shapes=[pltpu.VMEM((N_HEADS, t, LANES), F32), pltpu.VMEM((N_HEADS, t, LANES), F32),
                            pltpu.VMEM((t, D_HALF), F32)]),
        out_shape=[jax.ShapeDtypeStruct((s, D_HALF), F32), jax.ShapeDtypeStruct((N_HEADS, s, LANES), F32)],
        compiler_params=_params("arbitrary"),
    )(qi, kj, q, k, v, cc, cr)


def _causal_tile(t):
    return lax.broadcasted_iota(jnp.int32, (t, t), 0) >= lax.broadcasted_iota(jnp.int32, (t, t), 1)


def _wide(x):
    return jnp.concatenate([x, x], axis=1)


def _attn_probs(q2, k2, v2, do2, msk, causal, bias, lse_rows):
    zero = jnp.zeros_like(q2)
    qh = jnp.where(msk, q2, zero)
    doh = jnp.where(msk, do2, zero)
    sc = _dot_nt(qh, k2) + bias
    if causal is not None:
        sc = jnp.where(causal, sc, NEG)
    pm = jnp.exp(sc - _wide(lse_rows))
    dp = _dot_nt(doh, v2)
    return qh, doh, pm, dp


def _attn_bwd_rowdot(q, k, v, do, lse, cc, cr):
    s = q.shape[0]
    t = ATT_T
    nblk = s // t

    def body(qi_ref, kj_ref, q_ref, k_ref, v_ref, do_ref, lse_ref, cc_ref, cr_ref, dd_ref, acc):
        i = qi_ref[pl.program_id(0)]
        j = kj_ref[pl.program_id(0)]

        @pl.when(j == 0)
        def _():
            acc[...] = jnp.zeros_like(acc)

        def tile(on_diagonal):
            causal = _causal_tile(t) if on_diagonal else None
            left = lax.broadcasted_iota(jnp.int32, (1, LANES), 1) < HEAD
            for p in range(N_PAIRS):
                lanes = slice(p * LANES, (p + 1) * LANES)
                q2, k2, v2, do2 = q_ref[:, lanes], k_ref[:, lanes], v_ref[:, lanes], do_ref[:, lanes]
                for e in range(2):
                    h = 2 * p + e
                    msk = left if e == 0 else jnp.logical_not(left)
                    bias = _wide(cc_ref[h]) - cr_ref[h:h + 1, :]
                    _, _, pm, dp = _attn_probs(q2, k2, v2, do2, msk, causal, bias, lse_ref[h])
                    acc[h] += jnp.sum(pm * dp, axis=1, keepdims=True)

        pl.when(j < i)(functools.partial(tile, False))
        pl.when(j == i)(functools.partial(tile, True))

        @pl.when(j == i)
        def _():
            dd_ref[...] = acc[...]

    qi, kj = _tiles(nblk, by_query=True)
    qblk = pl.BlockSpec((t, D_HALF), lambda n, qi, kj: (qi[n], 0))
    qcol = pl.BlockSpec((N_HEADS, t, LANES), lambda n, qi, kj: (0, qi[n], 0))
    kblk = pl.BlockSpec((t, D_HALF), lambda n, qi, kj: (kj[n], 0))
    return pl.pallas_call(
        body, name="fox_attn_rowdot",
        grid_spec=pltpu.PrefetchScalarGridSpec(
            num_scalar_prefetch=2, grid=(qi.shape[0],),
            in_specs=[qblk, kblk, kblk, qblk, qcol, qcol, pl.BlockSpec((N_HEADS, t), lambda n, qi, kj: (0, kj[n]))],
            out_specs=qcol, scratch_shapes=[pltpu.VMEM((N_HEADS, t, LANES), F32)]),
        out_shape=jax.ShapeDtypeStruct((N_HEADS, s, LANES), F32),
        compiler_params=_params("arbitrary"),
    )(qi, kj, q, k, v, do, lse, cc, cr)


def _attn_bwd(q, k, v, do, lse, dd, cc, cr):
    s = q.shape[0]
    t = ATT_T
    nblk = s // t

    def body(qi_ref, kj_ref, q_ref, k_ref, v_ref, do_ref, lse_ref, dd_ref, cc_ref, cr_ref,
             dq_ref, dk_ref, dv_ref, dcr_ref, dk_sc, dv_sc, dcr_sc):
        i = qi_ref[pl.program_id(0)]
        j = kj_ref[pl.program_id(0)]

        @pl.when(pl.program_id(0) == 0)
        def _():
            dq_ref[...] = jnp.zeros_like(dq_ref)

        @pl.when(i == j)
        def _():
            dk_sc[...] = jnp.zeros_like(dk_sc)
            dv_sc[...] = jnp.zeros_like(dv_sc)
            dcr_sc[...] = jnp.zeros_like(dcr_sc)

        def tile(on_diagonal):
            causal = _causal_tile(t) if on_diagonal else None
            left = lax.broadcasted_iota(jnp.int32, (1, LANES), 1) < HEAD
            qrows = pl.ds(pl.multiple_of(i * t, t), t)
            for p in range(N_PAIRS):
                lanes = slice(p * LANES, (p + 1) * LANES)
                q2, k2, v2, do2 = q_ref[:, lanes], k_ref[:, lanes], v_ref[:, lanes], do_ref[:, lanes]
                zero = jnp.zeros_like(q2)
                dq2 = jnp.zeros((t, LANES), F32)
                dk2 = jnp.zeros((t, LANES), F32)
                dv2 = jnp.zeros((t, LANES), F32)
                for e in range(2):
                    h = 2 * p + e
                    msk = left if e == 0 else jnp.logical_not(left)
                    bias = _wide(cc_ref[h]) - cr_ref[h:h + 1, :]
                    qh, doh, pm, dp = _attn_probs(q2, k2, v2, do2, msk, causal, bias, lse_ref[h])
                    dsc = pm * (dp - _wide(dd_ref[h]))
                    dsb = dsc.astype(BF16)
                    dv2 += _dot_tn(pm.astype(BF16), doh)
                    dk2 += _dot_tn(dsb, qh)
                    dq2 += jnp.dot(dsb, jnp.where(msk, k2, zero), preferred_element_type=F32)
                    dcr_sc[h:h + 1, :] += -_colsum(dsc)
                dq_ref[qrows, lanes] += dq2 * ATT_SCALE
                dk_sc[:, lanes] += dk2
                dv_sc[:, lanes] += dv2

        pl.when(i > j)(functools.partial(tile, False))
        pl.when(i == j)(functools.partial(tile, True))

        @pl.when(i == nblk - 1)
        def _():
            dk_ref[...] = dk_sc[...]
            dv_ref[...] = dv_sc[...]
            dcr_ref[...] = dcr_sc[...]

    qi, kj = _tiles(nblk, by_query=False)
    qblk = pl.BlockSpec((t, D_HALF), lambda n, qi, kj: (qi[n], 0))
    qcol = pl.BlockSpec((N_HEADS, t, LANES), lambda n, qi, kj: (0, qi[n], 0))
    kblk = pl.BlockSpec((t, D_HALF), lambda n, qi, kj: (kj[n], 0))
    krow = pl.BlockSpec((N_HEADS, t), lambda n, qi, kj: (0, kj[n]))
    return pl.pallas_call(
        body, name="fox_attn_bwd",
        grid_spec=pltpu.PrefetchScalarGridSpec(
            num_scalar_prefetch=2, grid=(qi.shape[0],),
            in_specs=[qblk, kblk, kblk, qblk, qcol, qcol, qcol, krow],
            out_specs=[pl.BlockSpec((s, D_HALF), lambda n, qi, kj: (0, 0)), kblk, kblk, krow],
            scratch_shapes=[pltpu.VMEM((t, D_HALF), F32), pltpu.VMEM((t, D_HALF), F32), pltpu.VMEM((N_HEADS, t), F32)]),
        out_shape=[jax.ShapeDtypeStruct((s, D_HALF), F32)] * 3 + [jax.ShapeDtypeStruct((N_HEADS, s), F32)],
        compiler_params=_params("arbitrary"),
    )(qi, kj, q, k, v, do, lse, dd, cc, cr)


def _fox_prep_bwd(u_b, h_t, dq, dk, dv, dgate, dcum, fb, qg, kg, tm=256):
    s = u_b.shape[0]
    nb = s // tm
    d = h_t.shape[0]

    def body(ub_ref, ht_ref, dq_ref, dk_ref, dv_ref, dg_ref, dc_ref, fb_ref, qg_ref, kg_ref,
             du_ref, dwb_ref, dqg_ref, dkg_ref, dfb_ref, carry):
        i = pl.program_id(0)

        @pl.when(i == 0)
        def _():
            carry[...] = jnp.zeros_like(carry)
            dwb_ref[...] = jnp.zeros_like(dwb_ref)
            dqg_ref[...] = jnp.zeros_like(dqg_ref)
            dkg_ref[...] = jnp.zeros_like(dkg_ref)
            dfb_ref[...] = jnp.zeros_like(dfb_ref)

        bd = _head_ones()
        for lo, g_ref, d_ref, dgain_ref in ((0, qg_ref, dq_ref, dqg_ref), (512, kg_ref, dk_ref, dkg_ref)):
            gain = g_ref[...]
            xh, rinv, _ = _head_rms(ub_ref[:, lo:lo + 512], gain, bd)
            dn = d_ref[...]
            dgain_ref[...] += _colsum(dn * xh)
            dxh = dn * gain
            du_ref[:, lo:lo + 512] = rinv * (dxh - xh * (_head_sum(dxh * xh, bd) * (1.0 / HEAD)))
        du_ref[:, 1024:1536] = dv_ref[...]
        du_ref[:, 1536:2048] = dg_ref[...]
        lane = lax.broadcasted_iota(jnp.int32, (1, LANES), 1)
        dc = dc_ref[...]
        dlogf = _exact_dot(dc, _tri(tm, False), ones_first=True) + carry[...]
        carry[...] += _colsum(dc)
        fl = ub_ref[:, 2048:2176] + fb_ref[...]
        dfl = jnp.where(lane < N_HEADS, dlogf * (1.0 - _sigmoid(fl)), 0.0)
        du_ref[:, 2048:2176] = dfl
        dfb_ref[...] += _colsum(dfl)
        dwb_ref[...] += jnp.dot(ht_ref[...], du_ref[...].astype(BF16), preferred_element_type=F32)

    rev = lambda w: pl.BlockSpec((tm, w), lambda i: (nb - 1 - i, 0))
    vec = lambda w: pl.BlockSpec((1, w), lambda i: (0, 0))
    return pl.pallas_call(
        body, name="fox_prep_bwd", grid=(nb,),
        in_specs=[rev(SEC), pl.BlockSpec((d, tm), lambda i: (0, nb - 1 - i))] + [rev(D_HALF)] * 4
                 + [rev(LANES), vec(LANES), vec(D_HALF), vec(D_HALF)],
        out_specs=[rev(SEC), pl.BlockSpec((d, SEC), lambda i: (0, 0)), vec(D_HALF), vec(D_HALF), vec(LANES)],
        out_shape=[jax.ShapeDtypeStruct((s, SEC), F32), jax.ShapeDtypeStruct((d, SEC), F32),
                   jax.ShapeDtypeStruct((1, D_HALF), F32), jax.ShapeDtypeStruct((1, D_HALF), F32),
                   jax.ShapeDtypeStruct((1, LANES), F32)],
        scratch_shapes=[pltpu.VMEM((1, LANES), F32)],
        compiler_params=_params("arbitrary"),
    )(u_b, h_t, dq, dk, dv, dgate, dcum, fb, qg, kg)


def _merge(y, r, k, v, gate_a, o, u_b, h, x, tgt, w_g, wa, wb, wo, fg, lw, lb, rk, tm=256):
    s, d = x.shape

    def body(y_ref, r_ref, k_ref, v_ref, ga_ref, o_ref, gb_ref, h_ref, x_ref, t_ref, wg_ref, wa_ref, wb_ref, wo_ref,
             fg_ref, lw_ref, lb_ref, rk_ref,
             dx2_ref, dy_ref, drb_ref, dkb_ref, dvb_ref, dga_ref, do_ref, dgb_ref, dug_ref,
             dwa_ref, dwb_ref, dwo_ref, dfg_ref, loss_ref, dlw_ref, dlb_ref, drk_ref):
        i = pl.program_id(0)

        @pl.when(i == 0)
        def _():
            for ref in (dwa_ref, dwb_ref, dwo_ref, dfg_ref, loss_ref, dlw_ref, dlb_ref, drk_ref):
                ref[...] = jnp.zeros_like(ref)

        bd = _head_ones()
        wa_v, wb_v, wo_v, fg_v = wa_ref[...], wb_ref[...], wo_ref[...], fg_ref[...]
        rv, kv, vv, ga, lw_v, rk_v = r_ref[...], k_ref[...], v_ref[...], ga_ref[...], lw_ref[...], rk_ref[...]
        yn, rstd, rkk, sga, pre = _rwkv_post_math(y_ref[...], rv, kv, vv, ga, lw_v, lb_ref[...], rk_v, bd)
        silu_a = ga * sga
        gb, ov = gb_ref[...], o_ref[...]
        sgb = _sigmoid(gb)
        silu_b = gb * sgb
        ma = (pre * silu_a).astype(BF16)
        mb = (ov * silu_b).astype(BF16)
        ya = jnp.dot(ma, wa_v, preferred_element_type=F32)
        yb = jnp.dot(mb, wb_v, preferred_element_type=F32)
        ug = jnp.dot(h_ref[...], wg_ref[...], preferred_element_type=F32)
        sa = _sigmoid(ug[:, 0:d])
        sb = _sigmoid(ug[:, d:2 * d])
        merged = (sa * ya + sb * yb).astype(BF16)
        x2 = x_ref[...] + jnp.dot(merged, wo_v, preferred_element_type=F32)
        r2 = lax.rsqrt(jnp.mean(x2 * x2, axis=-1, keepdims=True) + RMS_EPS)
        x2h = x2 * r2
        err = x2h * fg_v - t_ref[...]
        loss_ref[...] += _colsum(err * err)
        dyo = err * (1.0 / d)
        dfg_ref[...] += _colsum(dyo * x2h)
        dx2h = dyo * fg_v
        dx2 = r2 * (dx2h - x2h * jnp.mean(dx2h * x2h, axis=-1, keepdims=True))
        dx2_ref[...] = dx2
        dx2b = dx2.astype(BF16)
        dmerged = _dot_nt(dx2b, wo_v)
        dwo_ref[...] += _dot_tn(merged, dx2b)
        dya = dmerged * sa
        dyb = dmerged * sb
        dug_ref[:, 0:d] = dya * ya * (1.0 - sa)
        dug_ref[:, d:2 * d] = dyb * yb * (1.0 - sb)
        dyab = dya.astype(BF16)
        dybb = dyb.astype(BF16)
        dwa_ref[...] += _dot_tn(ma, dyab)
        dwb_ref[...] += _dot_tn(mb, dybb)
        dmb = _dot_nt(dybb, wb_v)
        do_ref[...] = (dmb * silu_b).astype(BF16)
        dgb_ref[...] = dmb * ov * (sgb * (1.0 + gb * (1.0 - sgb)))
        dma = _dot_nt(dyab, wa_v)
        dga_ref[...] = dma * pre * (sga * (1.0 + ga * (1.0 - sga)))
        dpre = dma * silu_a
        dlw_ref[...] += _colsum(dpre * yn)
        dlb_ref[...] += _colsum(dpre)
        dyn = dpre * lw_v
        m1 = _head_sum(dyn, bd) * (1.0 / HEAD)
        m2 = _head_sum(dyn * yn, bd) * (1.0 / HEAD)
        dy_ref[...] = rstd * (dyn - m1 - yn * m2)
        dvb_ref[...] = dpre * rkk
        drkk = _head_sum(dpre * vv, bd)
        drb_ref[...] = drkk * kv * rk_v
        dkb_ref[...] = drkk * rv * rk_v
        drk_ref[...] += _colsum(drkk * rv * kv)

    row = lambda w: pl.BlockSpec((tm, w), lambda i: (i, 0))
    full = lambda a: pl.BlockSpec(a.shape, lambda i: (0, 0))
    once = lambda a: pl.BlockSpec(a.shape, lambda i: (0, 0), pipeline_mode=pl.Buffered(1))
    half = jax.ShapeDtypeStruct((s, D_HALF), F32)
    fshape = lambda a: jax.ShapeDtypeStruct(a.shape, F32)
    return pl.pallas_call(
        body, name="merge_fwd_bwd", grid=(s // tm,),
        in_specs=[row(D_HALF)] * 6 + [pl.BlockSpec((tm, D_HALF), lambda i: (i, 3)), row(d), row(d), row(d),
                                      once(w_g), once(wa), once(wb), once(wo), full(fg), full(lw), full(lb), full(rk)],
        out_specs=[row(d)] + [row(D_HALF)] * 7 + [row(GATE_COLS), full(wa), full(wb), full(wo), full(fg), full(fg),
                                                   full(lw), full(lb), full(rk)],
        out_shape=[jax.ShapeDtypeStruct((s, d), F32)] + [half] * 5 + [jax.ShapeDtypeStruct((s, D_HALF), BF16), half,
                                                                    jax.ShapeDtypeStruct((s, GATE_COLS), F32),
                                                                    fshape(wa), fshape(wb), fshape(wo), fshape(fg),
                                                                    fshape(fg), fshape(lw), fshape(lb), fshape(rk)],
        compiler_params=_params("arbitrary"),
    )(y, r, k, v, gate_a, o, u_b, h, x, tgt, w_g, wa, wb, wo, fg, lw, lb, rk)


def _lora_weight(w_up, a_up):
    z = jnp.zeros((LORA, D_HALF), w_up.dtype)
    return jnp.concatenate([jnp.concatenate([w_up, z], axis=1), jnp.concatenate([z, a_up], axis=1)], axis=0)


def _device_grads(x, tgt, p, w_a, w_up, a_up, late_weights, fwd_exchange=None, bwd_exchange=None, tail_exchange=None):
    wl = _lora_weight(w_up, a_up)
    rk = p["r_k"].reshape(1, D_HALF)
    fb = jnp.pad(p["f_bias"], ((0, 0), (0, LANES - N_HEADS)))
    qg = jnp.tile(p["q_norm_g"], (1, N_HEADS))
    kg = jnp.tile(p["k_norm_g"], (1, N_HEADS))
    fg = p["final_norm_g"].reshape(1, D_MODEL)
    mixer = (p["shift_mu"], wl, p["w0"], p["a0"], p["k_k"], p["k_a"])

    h, u_a, r, dec, k, v, av, bv, gate_a = _rwkv_front(x, p["norm_g"], w_a, *mixer)
    y, st, arrived = _wkv_fwd(r, dec, k, av, bv, v, fwd_exchange)

    w_b, w_g, w_out_a, w_out_b, w_out = late_weights(arrived)
    u_b, q, kn, vb, cc, cr = _fox_front(h, w_b, fb, qg, kg)
    o, lse = _attn_fwd(q, kn, vb, cc, cr)

    (dx2, dy, dr_b, dk_b, dv_b, dgate_a, do, dgate_b, du_g, dwa, dwb, dwo, dfg, loss_vec, dlw, dlb, drk) = _merge(
        y, r, k, v, gate_a, o, u_b, h, x, tgt, w_g, w_out_a, w_out_b, w_out, fg, p["lnx_w"], p["lnx_b"], rk)

    dd = _attn_bwd_rowdot(q, kn, vb, do, lse, cc, cr)
    dq, dk_att, dv_att, dcr = _attn_bwd(q, kn, vb, do, lse, dd, cc, cr)
    dcum = jnp.pad(dcr.T, ((0, 0), (0, LANES - N_HEADS)))
    h_t = h.T
    du_b, dw_b, dqg, dkg, dfb = _fox_prep_bwd(u_b, h_t, dq, dk_att, dv_att, dgate_b, dcum, fb, qg, kg)
    dw_g = _matmul_tn_acc(h_t, du_g, "dw_gate")

    scan_grads, sent = _wkv_bwd(r, dec, k, av, bv, v, dy, st,
                                bwd_exchange(dw_b, dw_g, dwa, dwb, dwo) if bwd_exchange else None)
    du_a, dw_a, dmu, dwl, dw0, da0, dkkw, dkaw = _rwkv_prep_bwd(
        u_a, h_t, (*scan_grads, dr_b, dk_b, dv_b, dgate_a), *mixer)
    dw_up, da_up = dwl[:LORA, :D_HALF], dwl[LORA:, D_HALF:]
    sent_last = _run_on_sequencer(tail_exchange(dw_a, dw_up, da_up), "scatter_tail", 1) if tail_exchange else []
    grad_x, dnorm_g, _ = _inproj_bwd(du_a, du_b, du_g, w_a, w_b, w_g, x, dx2, p["norm_g"])

    grads = dict(
        norm_g=dnorm_g, w_in=(dw_a, dw_b, dw_g), shift_mu=dmu,
        w_lora_up=dw_up, w0=dw0, a_lora_up=da_up, a0=da0, k_k=dkkw, k_a=dkaw,
        r_k=drk.reshape(1, N_HEADS, HEAD), lnx_w=dlw, lnx_b=dlb, f_bias=dfb[:, :N_HEADS],
        q_norm_g=dqg.reshape(N_HEADS, HEAD).sum(axis=0, keepdims=True),
        k_norm_g=dkg.reshape(N_HEADS, HEAD).sum(axis=0, keepdims=True),
        w_out_a=dwa, w_out_b=dwb, w_out=dwo, final_norm_g=dfg.reshape(D_MODEL))
    return loss_vec, grad_x, grads, sent, sent_last


CHIP_FLIPS = ((1, 0), (0, 1), (1, 1))
ANY = pl.BlockSpec(memory_space=pl.ANY)


def _position():
    return lax.axis_index("x"), lax.axis_index("y"), lax.axis_index("c")


def _flip(v, f):
    return 1 - v if f else v


def _both(a, b):
    if a is None:
        return b
    return a if b is None else jnp.logical_and(a, b)


def _when(cond, fn):
    if cond is None:
        fn()
    else:
        pl.when(cond)(fn)


class _Moves:
    def __init__(self, send_sems, recv_sems, local_sems):
        self.send_sems, self.recv_sems, self.local_sems = send_sems, recv_sems, local_sems
        self.remote, self.local = [], []

    def send(self, src, dst, peer, landing, send_if=None, recv_if=None, first=False):
        k = len(self.remote)
        sems = dict(send_sem=self.send_sems.at[k], recv_sem=self.recv_sems.at[k], device_id=peer, device_id_type=MESH)
        out = pltpu.make_async_remote_copy(src_ref=src, dst_ref=dst, **sems)
        arrival = pltpu.make_async_remote_copy(src_ref=src, dst_ref=landing, **sems)
        self.remote.append((out, arrival, send_if, recv_if, first))

    def copy(self, src, dst, cond=None):
        cp = pltpu.make_async_copy(src, dst, self.local_sems.at[len(self.local)])
        self.local.append((cp, cond))

    def start(self, also=None):
        for cp, cond in self.local:
            _when(_both(also, cond), cp.start)
        for out, _, send_if, _, _ in self.remote:
            _when(_both(also, send_if), out.start)

    def wait_arrivals(self, also=None, first=None):
        for _, arrival, _, recv_if, is_first in self.remote:
            if first is None or first == is_first:
                _when(_both(also, recv_if), arrival.wait_recv)

    def wait_sent(self, also=None):
        for out, _, send_if, _, _ in self.remote:
            _when(_both(also, send_if), out.wait_send)
        for cp, cond in self.local:
            _when(_both(also, cond), cp.wait)

    def wait(self, also=None):
        self.wait_arrivals(also)
        self.wait_sent(also)


class _Exchange:
    def __init__(self, operands, out_shapes, n_remote, n_local, build, relays=None, in_place=(), n_staging=0):
        self.operands, self.out_shapes = list(operands), list(out_shapes)
        self.n_remote, self.n_local, self.build = n_remote, n_local, build
        self.relays, self.in_place = relays, in_place
        self.n_staging = n_staging

    def scratch(self):
        return [pltpu.SemaphoreType.DMA((self.n_remote,)), pltpu.SemaphoreType.DMA((self.n_remote,)),
                pltpu.SemaphoreType.DMA((max(self.n_local, 1),))]

    def moves(self, in_refs, out_refs, sems):
        mv = _Moves(*sems)
        self.build(mv, in_refs, out_refs)
        return mv


def _run_on_sequencer(exchange, name, collective_id):
    ins = [jax.new_ref(a, memory_space=pltpu.MemorySpace.HBM) for a in exchange.operands]
    outs = [ins[i] if i in exchange.in_place else jax.empty_ref(s, memory_space=pltpu.MemorySpace.HBM)
            for i, s in enumerate(exchange.out_shapes)]
    forward, to_sibling = exchange.relays or (None, None)
    relay_scratch = [pltpu.SemaphoreType.DMA((stage[0],)) for stage in (forward, to_sibling) if stage for _ in range(2)]

    def launch(*sems):
        x, y, c = _position()
        peers = [(_flip(x, fx), _flip(y, fy), c) for fx, fy in CHIP_FLIPS] + ([(x, y, 1 - c)] if to_sibling else [])
        barrier = pltpu.get_barrier_semaphore()
        for peer in peers:
            pl.semaphore_signal(barrier, inc=1, device_id=peer, device_id_type=MESH)
        pl.semaphore_wait(barrier, len(peers))
        moves = exchange.moves(ins, outs, sems[:3])
        moves.start()
        later = []
        if forward:
            onward = _Moves(sems[3], sems[4], None)
            forward[1](onward, ins, outs)
            moves.wait_arrivals(first=True)
            onward.start()
            moves.wait_arrivals(first=False)
            onward.wait_arrivals()
            later.append(onward)
        else:
            moves.wait_arrivals()
        if to_sibling:
            passed = _Moves(*sems[-2:], None)
            to_sibling[1](passed, ins, outs)
            passed.start()
            passed.wait_arrivals()
            later.append(passed)
        for mv in later + [moves]:
            mv.wait_sent()

    pl.kernel(launch, mesh=plsc.ScalarSubcoreMesh(axis_name="sequencer", num_cores=1), name=name,
              scratch_types=tuple(exchange.scratch() + relay_scratch),
              compiler_params=pltpu.CompilerParams(collective_id=collective_id))()
    return [o[...] for o in outs[:len(outs) - exchange.n_staging]]


def _row_major_copy(a, name):
    r, c = a.shape
    tr = _row_tile(r)

    def body(a_ref, o_ref):
        o_ref[...] = a_ref[...]

    blk = pl.BlockSpec((tr, c), lambda i: (i, 0))
    return pl.pallas_call(body, name=name, grid=(r // tr,), in_specs=[blk], out_specs=blk,
                          out_shape=jax.ShapeDtypeStruct(a.shape, a.dtype), compiler_params=_params("parallel"))(a)


def _is_chip(x, y, chip):
    return jnp.logical_and(x == chip // 2, y == chip % 2)


def _gather_exchange(from_chip, from_all, split=()):
    n1, n2 = len(from_chip), len(from_all)
    near = CHIP_FLIPS[:2]

    def quarters(t, c, first, count=1):
        n = from_chip[t][1].shape[0] // 4
        return pl.ds((2 * c + first) * n, count * n)

    def build(mv, ins, outs):
        x, y, c = _position()
        me = 2 * x + y
        for t, (chip, _) in enumerate(from_chip):
            if t not in split:
                mv.copy(ins[t], outs[t], cond=_is_chip(x, y, chip))
        for t in range(n2):
            mv.copy(ins[n1 + t], outs[n1 + t].at[me])
        for t in split:
            for first in (True, False):
                for f, (fx, fy) in enumerate(near):
                    px, py = _flip(x, fx), _flip(y, fy)
                    part = quarters(t, c, f if first else 1 - f)
                    mv.send(ins[t].at[part], outs[t].at[part], (px, py, c), landing=outs[t].at[part], first=first,
                            send_if=_is_chip(x, y, from_chip[t][0]), recv_if=_is_chip(px, py, from_chip[t][0]))
        for fx, fy in CHIP_FLIPS:
            px, py = _flip(x, fx), _flip(y, fy)
            peer = (px, py, c)
            for t, (chip, _) in enumerate(from_chip):
                if t not in split:
                    mv.send(ins[t], outs[t], peer, landing=outs[t],
                            send_if=_is_chip(x, y, chip), recv_if=_is_chip(px, py, chip))
            for t in range(n2):
                mv.send(ins[n1 + t], outs[n1 + t].at[me], peer, landing=outs[n1 + t].at[2 * px + py])

    def forward(mv, ins, outs):
        x, y, c = _position()
        for t in split:
            chip = from_chip[t][0]
            for f, (fx, fy) in enumerate(near):
                gx, gy = near[1 - f]
                part = quarters(t, c, f)
                mv.send(outs[t].at[part], outs[t].at[part], (_flip(x, gx), _flip(y, gy), c), landing=outs[t].at[part],
                        send_if=_is_chip(_flip(x, fx), _flip(y, fy), chip), recv_if=_is_chip(1 - x, 1 - y, chip))

    def to_sibling(mv, ins, outs):
        x, y, c = _position()
        for t in split:
            came = jnp.logical_not(_is_chip(x, y, from_chip[t][0]))
            mv.send(outs[t].at[quarters(t, c, 0, 2)], outs[t].at[quarters(t, c, 0, 2)], (x, y, 1 - c),
                    landing=outs[t].at[quarters(t, 1 - c, 0, 2)], send_if=came, recv_if=came)

    arrays = [a for _, a in from_chip] + list(from_all)
    shapes = [jax.ShapeDtypeStruct(a.shape, a.dtype) for _, a in from_chip]
    shapes += [jax.ShapeDtypeStruct((N_CHIPS,) + a.shape, a.dtype) for a in from_all]
    n_remote = len(CHIP_FLIPS) * (n1 - len(split) + n2) + 2 * len(near) * len(split)
    relays = ((len(near) * len(split), forward), (len(split), to_sibling)) if split else None
    return _Exchange(arrays, shapes, n_remote, n1 + n2, build, relays, in_place=split)


def _scatter_exchange(to_chip, to_all, via_neighbours=False):
    n1, n2 = len(to_chip), len(to_all)
    near = CHIP_FLIPS[:2]
    direct = near if via_neighbours else CHIP_FLIPS

    def half(t, g):
        n = to_chip[t][1].shape[0] // 2
        return pl.ds(g * n, n)

    def build(mv, ins, outs):
        x, y, c = _position()
        if via_neighbours:
            for t, (chip, _) in enumerate(to_chip):
                for g, (gx, gy) in enumerate(near):
                    ox, oy = near[1 - g]
                    mv.send(ins[t].at[half(t, g)], outs[n1 + n2 + t], (_flip(x, gx), _flip(y, gy), c),
                            landing=outs[n1 + n2 + t], first=True, send_if=_is_chip(1 - x, 1 - y, chip),
                            recv_if=_is_chip(_flip(x, ox), _flip(y, oy), chip))
        for f, (fx, fy) in enumerate(CHIP_FLIPS):
            px, py = _flip(x, fx), _flip(y, fy)
            peer = (px, py, c)
            if (fx, fy) in direct:
                for t, (chip, _) in enumerate(to_chip):
                    mv.send(ins[t], outs[t].at[f], peer, landing=outs[t].at[f],
                            send_if=_is_chip(px, py, chip), recv_if=_is_chip(x, y, chip))
            for t in range(n2):
                mv.send(ins[n1 + t].at[2 * px + py], outs[n1 + t].at[f], peer, landing=outs[n1 + t].at[f])

    def forward(mv, ins, outs):
        x, y, c = _position()
        for t, (chip, _) in enumerate(to_chip):
            for g in range(len(near)):
                ox, oy = near[1 - g]
                far_slot = outs[t].at[len(near)].at[half(t, g)]
                mv.send(outs[n1 + n2 + t], far_slot, (_flip(x, ox), _flip(y, oy), c), landing=far_slot,
                        send_if=_is_chip(_flip(x, ox), _flip(y, oy), chip), recv_if=_is_chip(x, y, chip))

    arrays = [a for _, a in to_chip] + list(to_all)
    shapes = [jax.ShapeDtypeStruct((len(CHIP_FLIPS),) + a.shape, a.dtype) for _, a in to_chip]
    shapes += [jax.ShapeDtypeStruct((len(CHIP_FLIPS),) + a.shape[1:], a.dtype) for a in to_all]
    if not via_neighbours:
        return _Exchange(arrays, shapes, len(CHIP_FLIPS) * (n1 + n2), 0, build)
    shapes += [jax.ShapeDtypeStruct((a.shape[0] // 2, a.shape[1]), a.dtype) for _, a in to_chip]
    return _Exchange(arrays, shapes, 2 * len(near) * n1 + len(CHIP_FLIPS) * n2, 0, build,
                     relays=((len(near) * n1, forward), None), n_staging=n1)


def _swap_sibling(tensors, name):
    n = len(tensors)

    def body(*refs):
        ins, outs = refs[:n], refs[n:2 * n]
        send_sems, recv_sems = refs[2 * n:]
        x, y, c = _position()
        copies = [pltpu.make_async_remote_copy(
            src_ref=ins[t], dst_ref=outs[t], send_sem=send_sems.at[t], recv_sem=recv_sems.at[t],
            device_id=(x, y, 1 - c), device_id_type=MESH) for t in range(n)]
        for cp in copies:
            cp.start()
        for cp in copies:
            cp.wait_recv()
        for cp in copies:
            cp.wait_send()

    return pl.pallas_call(
        body, name=name, in_specs=[ANY] * n, out_specs=[ANY] * n,
        out_shape=[jax.ShapeDtypeStruct(a.shape, a.dtype) for a in tensors],
        scratch_shapes=[pltpu.SemaphoreType.DMA((n,)), pltpu.SemaphoreType.DMA((n,))],
        compiler_params=pltpu.CompilerParams(has_side_effects=True),
    )(*tensors)


def _pair_halves(g):
    r, cols = g.shape
    half = r // 2

    def body(g_ref, o_ref, mine, theirs, send_sem, recv_sem, local_sem):
        x, y, c = _position()
        away = pltpu.make_async_remote_copy(
            src_ref=g_ref.at[pl.ds((1 - c) * half, half)], dst_ref=theirs, send_sem=send_sem, recv_sem=recv_sem,
            device_id=(x, y, 1 - c), device_id_type=MESH)
        kept = pltpu.make_async_copy(g_ref.at[pl.ds(c * half, half)], mine, local_sem)
        away.start()
        kept.start()
        kept.wait()
        away.wait_recv()
        o_ref[...] = (mine[...] + theirs[...]).astype(BF16)
        away.wait_send()

    return pl.pallas_call(
        body, name="pair_halves", in_specs=[ANY], out_specs=pl.BlockSpec(memory_space=pltpu.VMEM),
        out_shape=jax.ShapeDtypeStruct((half, cols), BF16),
        scratch_shapes=[pltpu.VMEM((half, cols), F32), pltpu.VMEM((half, cols), F32),
                        pltpu.SemaphoreType.DMA(()), pltpu.SemaphoreType.DMA(()), pltpu.SemaphoreType.DMA(())],
        compiler_params=pltpu.CompilerParams(has_side_effects=True, vmem_limit_bytes=VMEM_LIMIT),
    )(g)


def _allreduce_small(slab):
    stages = 3

    def body(x_ref, o_ref, buf, send_sems, recv_sems):
        x, y, c = _position()
        peers = ((1 - x, y, c), (x, 1 - y, c), (x, y, 1 - c))
        o_ref[...] = x_ref[...]
        for k, peer in enumerate(peers):
            cp = pltpu.make_async_remote_copy(src_ref=o_ref, dst_ref=buf.at[k], send_sem=send_sems.at[k],
                                              recv_sem=recv_sems.at[k], device_id=peer, device_id_type=MESH)
            cp.start()
            cp.wait()
            o_ref[...] = o_ref[...] + buf[k]

    return pl.pallas_call(
        body, name="allreduce_small",
        in_specs=[pl.BlockSpec(memory_space=pltpu.VMEM)], out_specs=pl.BlockSpec(memory_space=pltpu.VMEM),
        out_shape=jax.ShapeDtypeStruct(slab.shape, slab.dtype),
        scratch_shapes=[pltpu.VMEM((stages,) + slab.shape, slab.dtype),
                        pltpu.SemaphoreType.DMA((stages,)), pltpu.SemaphoreType.DMA((stages,))],
        compiler_params=pltpu.CompilerParams(has_side_effects=True),
    )(slab)


def _row_tile(r):
    return min(r, 128)


def _sum4(stack, recv, me):
    _, r, c = stack.shape
    tr = _row_tile(r)

    def body(me_ref, own_ref, recv_ref, o_ref):
        o_ref[...] = (((own_ref[...] + recv_ref[0].astype(F32)) + recv_ref[1].astype(F32))
                      + recv_ref[2].astype(F32))

    return pl.pallas_call(
        body, name="sum_partials",
        grid_spec=pltpu.PrefetchScalarGridSpec(
            num_scalar_prefetch=1, grid=(r // tr,),
            in_specs=[pl.BlockSpec((None, tr, c), lambda i, me_ref: (me_ref[0], i, 0)),
                      pl.BlockSpec((len(CHIP_FLIPS), tr, c), lambda i, me_ref: (0, i, 0))],
            out_specs=pl.BlockSpec((tr, c), lambda i, me_ref: (i, 0))),
        out_shape=jax.ShapeDtypeStruct((r, c), F32), compiler_params=_params("parallel"),
    )(me, stack, recv)


def _sum_block(own, recv):
    r, c = own.shape
    tr = _row_tile(r)

    def body(own_ref, recv_ref, o_ref):
        o_ref[...] = (((own_ref[...] + recv_ref[0].astype(F32)) + recv_ref[1].astype(F32))
                      + recv_ref[2].astype(F32))

    return pl.pallas_call(
        body, name="sum_block", grid=(r // tr,),
        in_specs=[pl.BlockSpec((tr, c), lambda i: (i, 0)), pl.BlockSpec((len(CHIP_FLIPS), tr, c), lambda i: (0, i, 0))],
        out_specs=pl.BlockSpec((tr, c), lambda i: (i, 0)),
        out_shape=jax.ShapeDtypeStruct((r, c), F32), compiler_params=_params("parallel"),
    )(own, recv)


def _sum_half(own, recv, core):
    r, c = own.shape
    tr = _row_tile(r // 2)
    per_half = r // 2 // tr

    def body(core_ref, own_ref, recv_ref, o_ref):
        mine = pl.program_id(0) // per_half == core_ref[0]

        @pl.when(mine)
        def _():
            o_ref[...] = (((own_ref[...] + recv_ref[0].astype(F32)) + recv_ref[1].astype(F32))
                          + recv_ref[2].astype(F32))

        @pl.when(jnp.logical_not(mine))
        def _():
            o_ref[...] = own_ref[...]

    return pl.pallas_call(
        body, name="sum_half",
        grid_spec=pltpu.PrefetchScalarGridSpec(
            num_scalar_prefetch=1, grid=(r // tr,),
            in_specs=[pl.BlockSpec((tr, c), lambda i, core: (i, 0)),
                      pl.BlockSpec((len(CHIP_FLIPS), tr, c), lambda i, core: (0, i % per_half, 0))],
            out_specs=pl.BlockSpec((tr, c), lambda i, core: (i, 0))),
        out_shape=jax.ShapeDtypeStruct((r, c), F32), compiler_params=_params("parallel"),
    )(core, own, recv)


def _adamw_math(w, g, m, v):
    m = ADAM_B1 * m + (1.0 - ADAM_B1) * g
    v = ADAM_B2 * v + (1.0 - ADAM_B2) * (g * g)
    m_hat = m / (1.0 - ADAM_B1 ** ADAM_STEP)
    v_hat = v / (1.0 - ADAM_B2 ** ADAM_STEP)
    delta = -ADAM_LR * (m_hat / (jnp.sqrt(v_hat) + ADAM_EPS) + ADAM_WD * w)
    return delta, m, v


def _adamw(w, m, v, g_parts, name):
    r, c = w.shape
    tr = _row_tile(r)
    n = len(g_parts)

    def body(*refs):
        w_ref, m_ref, v_ref = refs[:3]
        g_refs = refs[3:3 + n]
        g_out, d_out, m_out, v_out, zero_out = refs[3 + n:]
        g = g_refs[0][...]
        for ref in g_refs[1:]:
            g = g + ref[...]
        g_out[...] = g
        d_out[...], m_out[...], v_out[...] = _adamw_math(w_ref[...], g, m_ref[...], v_ref[...])
        zero_out[0] = 0

    blk = pl.BlockSpec((tr, c), lambda i: (i, 0))
    return pl.pallas_call(
        body, name=name, grid=(r // tr,), in_specs=[blk] * (3 + n),
        out_specs=[blk] * 4 + [pl.BlockSpec(memory_space=pltpu.SMEM)],
        out_shape=[jax.ShapeDtypeStruct((r, c), F32)] * 4 + [jax.ShapeDtypeStruct((1,), jnp.int32)],
        compiler_params=_params("arbitrary"),
    )(w, m, v, *g_parts)


def _adamw_small(total, w, m, v):
    sizes = [w[n].size for n in SMALL]
    flat = lambda d: [d[n].reshape(1, -1) for n in SMALL]
    k = len(SMALL)

    def body(*refs):
        total_ref, w_refs, m_refs, v_refs = refs[0], refs[1:1 + k], refs[1 + k:1 + 2 * k], refs[1 + 2 * k:1 + 3 * k]
        outs = refs[1 + 3 * k:]
        for i, size in enumerate(sizes):
            g = total_ref[i:i + 1, 0:size]
            outs[i][...] = g
            outs[k + i][...], outs[2 * k + i][...], outs[3 * k + i][...] = _adamw_math(
                w_refs[i][...], g, m_refs[i][...], v_refs[i][...])

    res = pl.pallas_call(
        body, name="adamw_small", out_shape=[jax.ShapeDtypeStruct((1, size), F32) for size in sizes] * 4,
        compiler_params=_params(),
    )(total, *flat(w), *flat(m), *flat(v))
    return [{n: res[j * k + i].reshape(w[n].shape) for i, n in enumerate(SMALL)} for j in range(4)]


SHARDED = ("w_in", "w_lora_up", "a_lora_up", "w_out_a", "w_out_b", "w_out")
ROW_SHARDED = ("w_out",)
SMALL = ("norm_g", "shift_mu", "w0", "a0", "k_k", "k_a", "r_k", "lnx_w", "lnx_b", "f_bias", "q_norm_g", "k_norm_g",
         "final_norm_g")
WEIGHTS = ("norm_g", "w_in", "shift_mu", "w_lora_up", "w0", "a_lora_up", "a0", "k_k", "k_a", "r_k", "lnx_w", "lnx_b",
           "f_bias", "q_norm_g", "k_norm_g", "w_out_a", "w_out_b", "w_out", "final_norm_g")
SLAB_ROWS = 16
SLAB_COLS = SEC


def _to_slab(named, extra=None):
    rows = [jnp.pad(named[n].reshape(1, -1), ((0, 0), (0, SLAB_COLS - named[n].size))) for n in SMALL]
    if extra is not None:
        rows.append(jnp.pad(extra.reshape(1, -1), ((0, 0), (0, SLAB_COLS - extra.size))))
    rows.append(jnp.zeros((SLAB_ROWS - len(rows), SLAB_COLS), F32))
    return jnp.concatenate(rows, axis=0)


def _by_chip(g, name):
    if name in ROW_SHARDED:
        return g.reshape(N_CHIPS, g.shape[0] // N_CHIPS, g.shape[1])
    r, c = g.shape
    return g.reshape(r, N_CHIPS, c // N_CHIPS).transpose(1, 0, 2)


def _from_chips(stack, name):
    if name in ROW_SHARDED:
        return stack.reshape(-1, stack.shape[2])
    _, r, c = stack.shape
    return stack.transpose(1, 0, 2).reshape(r, N_CHIPS * c)


def kernel(x, norm_g, w_in, shift_mu, w_lora_up, w0, a_lora_up, a0, k_k, k_a, r_k, lnx_w, lnx_b, f_bias, q_norm_g, k_norm_g, w_out_a, w_out_b, w_out, final_norm_g, loss_target, m_norm_g, m_w_in, m_shift_mu, m_w_lora_up, m_w0, m_a_lora_up, m_a0, m_k_k, m_k_a, m_r_k, m_lnx_w, m_lnx_b, m_f_bias, m_q_norm_g, m_k_norm_g, m_w_out_a, m_w_out_b, m_w_out, m_final_norm_g, v_norm_g, v_w_in, v_shift_mu, v_w_lora_up, v_w0, v_a_lora_up, v_a0, v_k_k, v_k_a, v_r_k, v_lnx_w, v_lnx_b, v_f_bias, v_q_norm_g, v_k_norm_g, v_w_out_a, v_w_out_b, v_w_out, v_final_norm_g):
    w = dict(norm_g=norm_g, w_in=w_in, shift_mu=shift_mu, w_lora_up=w_lora_up, w0=w0, a_lora_up=a_lora_up, a0=a0,
             k_k=k_k, k_a=k_a, r_k=r_k, lnx_w=lnx_w, lnx_b=lnx_b, f_bias=f_bias, q_norm_g=q_norm_g,
             k_norm_g=k_norm_g, w_out_a=w_out_a, w_out_b=w_out_b, w_out=w_out, final_norm_g=final_norm_g)
    m = dict(norm_g=m_norm_g, w_in=m_w_in, shift_mu=m_shift_mu, w_lora_up=m_w_lora_up, w0=m_w0,
             a_lora_up=m_a_lora_up, a0=m_a0, k_k=m_k_k, k_a=m_k_a, r_k=m_r_k, lnx_w=m_lnx_w, lnx_b=m_lnx_b,
             f_bias=m_f_bias, q_norm_g=m_q_norm_g, k_norm_g=m_k_norm_g, w_out_a=m_w_out_a, w_out_b=m_w_out_b,
             w_out=m_w_out, final_norm_g=m_final_norm_g)
    v = dict(norm_g=v_norm_g, w_in=v_w_in, shift_mu=v_shift_mu, w_lora_up=v_w_lora_up, w0=v_w0,
             a_lora_up=v_a_lora_up, a0=v_a0, k_k=v_k_k, k_a=v_k_a, r_k=v_r_k, lnx_w=v_lnx_w, lnx_b=v_lnx_b,
             f_bias=v_f_bias, q_norm_g=v_q_norm_g, k_norm_g=v_k_norm_g, w_out_a=v_w_out_a, w_out_b=v_w_out_b,
             w_out=v_w_out, final_norm_g=v_final_norm_g)
    shapes = {n: w[n].shape for n in WEIGHTS}

    shard = {n: w[n][0].astype(BF16) for n in SHARDED}
    late = ("w_out_a", "w_out_b", "w_out")
    loras = ("w_lora_up", "a_lora_up")
    w_in_head, w_in_tail = shard["w_in"][:, :A_TAIL], shard["w_in"][:, A_TAIL:]
    shard0, shard1_head, up_stack, aup_stack = _run_on_sequencer(_gather_exchange(
        [(0, shard["w_in"]), (1, w_in_head)], [shard[n] for n in loras], split=(0, 1)), "gather_early", 2)
    moments = (_row_major_copy(m["w_in"][0], "m_w_in_rows"), _row_major_copy(v["w_in"][0], "v_w_in_rows"))
    shard0, moments = lax.optimization_barrier((shard0, moments))
    w_a = jnp.concatenate([shard0, shard1_head], axis=1)

    def late_weights(arrived):
        shard1_tail, shard2, shard3 = arrived[:3]
        w_b = jnp.concatenate([shard1_tail, shard2[:, :B_TAIL], jnp.zeros((D_MODEL, SEC - FOX_REAL), BF16)], axis=1)
        w_g = jnp.concatenate([shard2[:, B_TAIL:], shard3], axis=1)
        return (w_b, w_g, *[_from_chips(s, n) for n, s in zip(late, arrived[3:])])

    own = {}
    cut = {"block0": lambda: own["dw_a"][:, :SHARD_COLS], "head1": lambda: own["dw_a"][:, SHARD_COLS:],
           "tail1": lambda: own["dw_b"][:, :B_HEAD],
           "block2": lambda: jnp.concatenate([own["dw_b"][:, B_HEAD:FOX_REAL], own["dw_g"][:, :G_HEAD]], axis=1),
           "block3": lambda: own["dw_g"][:, G_HEAD:]}

    def bwd_exchange(dw_b, dw_g, dwa, dwb, dwo):
        own.update(dw_b=dw_b, dw_g=dw_g)
        own.update({n: _by_chip(g, n) for n, g in zip(late, (dwa, dwb, dwo))})
        return _scatter_exchange([(1, cut["tail1"]().astype(BF16)), (2, cut["block2"]().astype(BF16)),
                                  (3, cut["block3"]().astype(BF16))], [own[n].astype(BF16) for n in late])

    def tail_exchange(dw_a, dw_up, da_up):
        own.update(dw_a=dw_a)
        own.update({n: _by_chip(g, n) for n, g in zip(loras, (dw_up, da_up))})
        pair = _pair_halves(dw_a)
        return _scatter_exchange([(0, pair[:, :SHARD_COLS]), (1, pair[:, SHARD_COLS:])],
                                 [own[n].astype(BF16) for n in loras], via_neighbours=True)

    small = {n: w[n] for n in SMALL}
    loss_vec, grad_x, grads, sent, sent_last = _device_grads(
        x[0], loss_target[0], small, w_a, _from_chips(up_stack, "w_lora_up"), _from_chips(aup_stack, "a_lora_up"),
        late_weights, _gather_exchange([(1, w_in_tail), (2, shard["w_in"]), (3, shard["w_in"])], [shard[n] for n in late]),
        bwd_exchange, tail_exchange)

    total = _allreduce_small(_to_slab(grads, extra=loss_vec))
    loss = (0.5 / D_MODEL) * jnp.sum(total[len(SMALL)])
    out_g, out_d, out_m, out_v = _adamw_small(total, w, m, v)

    xpos, ypos, cpos = _position()
    me = (2 * xpos + ypos).astype(jnp.int32).reshape(1)
    core = cpos.astype(jnp.int32).reshape(1)
    core_sum, theirs = {}, {}

    def update(n):
        m_n, v_n = moments if n == "w_in" else (m[n][0], v[n][0])
        g, d, m2, v2, zero = _adamw(w[n][0], m_n, v_n, [core_sum[n], theirs[n]], "adamw_" + n)
        out_g[n], out_d[n], out_m[n], out_v[n] = (a.reshape(shapes[n]) for a in (g, d, m2, v2))
        return zero

    sent_last, out_d["norm_g"] = lax.optimization_barrier((sent_last, out_d["norm_g"]))
    core_sum["w_in"] = lax.switch(me[0], [
        lambda: _sum_half(cut["block0"](), sent_last[0], core),
        lambda: jnp.concatenate([_sum_half(cut["head1"](), sent_last[1], core), _sum_block(cut["tail1"](), sent[0])],
                                axis=1),
        lambda: _sum_block(cut["block2"](), sent[1]),
        lambda: _sum_block(cut["block3"](), sent[2])])
    core_sum.update({n: _sum4(own[n], r, me) for n, r in zip(loras, sent_last[2:])})
    rest = ("w_in",) + loras
    theirs.update(zip(rest, _swap_sibling([core_sum[n] for n in rest], "swap_sibling")))
    after_w_in = me + [update(n) for n in rest][0]
    core_sum.update({n: _sum4(own[n], r, after_w_in) for n, r in zip(late, sent[3:])})
    theirs.update(zip(late, _swap_sibling([core_sum[n] for n in late], "swap_sibling_late")))
    for n in late:
        update(n)

    return (loss, grad_x.reshape(x.shape), *[out_g[n] for n in WEIGHTS], *[out_d[n] for n in WEIGHTS],
            *[out_m[n] for n in WEIGHTS], *[out_v[n] for n in WEIGHTS])
```

```python
import functools
import math

import jax
import jax.numpy as jnp
from jax import lax
from jax.experimental import pallas as pl
from jax.experimental.pallas import tpu as pltpu
from jax.experimental.pallas import tpu_sc as plsc

F32 = jnp.float32
BF16 = jnp.bfloat16

D_MODEL = 1024
D_HALF = 512
HEAD = 64
N_HEADS = 8
LORA = 64
RWKV_COLS = 2176
FOX_REAL = 2056
SEC = 2176
GATE_COLS = 2048
IN_COLS = 6280
N_CHIPS = 4
SHARD_COLS = IN_COLS // N_CHIPS
A_TAIL = RWKV_COLS - SHARD_COLS
B_HEAD = SHARD_COLS - A_TAIL
B_TAIL = FOX_REAL - B_HEAD
G_HEAD = SHARD_COLS - B_TAIL
RMS_EPS = 1e-6
LNX_EPS = 64e-5
ATT_SCALE = HEAD ** -0.5
NEG = -1e30

ADAM_LR = 0.001
ADAM_B1 = 0.9
ADAM_B2 = 0.999
ADAM_EPS = 1e-08
ADAM_WD = 0.01
ADAM_STEP = 10

LANES = 128
SUBLANES = 8
VMEM_LIMIT = 56 * 1024 * 1024
MESH = pl.DeviceIdType.MESH


def _params(*sem):
    return pltpu.CompilerParams(dimension_semantics=sem if sem else None, vmem_limit_bytes=VMEM_LIMIT)


def _sigmoid(x):
    return 1.0 / (1.0 + jnp.exp(-x))


def _log_sigmoid(x):
    return jnp.minimum(x, 0.0) - jnp.log(1.0 + jnp.exp(-jnp.abs(x)))


def _head_ones():
    r = lax.broadcasted_iota(jnp.int32, (LANES, LANES), 0) >> 6
    c = lax.broadcasted_iota(jnp.int32, (LANES, LANES), 1) >> 6
    return (r == c).astype(BF16)


def _split3(x):
    hi = x.astype(BF16)
    r1 = x - hi.astype(F32)
    mid = r1.astype(BF16)
    lo = (r1 - mid.astype(F32)).astype(BF16)
    return hi, mid, lo


def _exact_dot(x, ones_bf16, ones_first=False):
    out = None
    for piece in _split3(x):
        if ones_first:
            t = jnp.dot(ones_bf16, piece, preferred_element_type=F32)
        else:
            t = jnp.dot(piece, ones_bf16, preferred_element_type=F32)
        out = t if out is None else out + t
    return out


def _head_sum(x, bd):
    n = x.shape[1] // LANES
    parts = [_exact_dot(x[:, i * LANES:(i + 1) * LANES], bd) for i in range(n)]
    return parts[0] if n == 1 else jnp.concatenate(parts, axis=1)


def _dot_nt(a, b):
    return lax.dot_general(a, b, (((1,), (1,)), ((), ())), preferred_element_type=F32)


def _dot_tn(a, b):
    return lax.dot_general(a, b, (((0,), (0,)), ((), ())), preferred_element_type=F32)


def _colsum(x):
    return jnp.sum(x, axis=0, keepdims=True)


def _matmul_tn_acc(at, b, name, tk=512):
    m, k = at.shape
    n = b.shape[1]

    def body(a_ref, b_ref, o_ref):
        j = pl.program_id(0)

        @pl.when(j == 0)
        def _():
            o_ref[...] = jnp.zeros_like(o_ref)

        o_ref[...] += jnp.dot(a_ref[...], b_ref[...], preferred_element_type=F32)

    return pl.pallas_call(
        body, name=name, grid=(k // tk,),
        in_specs=[pl.BlockSpec((m, tk), lambda j: (0, j)), pl.BlockSpec((tk, n), lambda j: (j, 0))],
        out_specs=pl.BlockSpec((m, n), lambda j: (0, 0)),
        out_shape=jax.ShapeDtypeStruct((m, n), F32), compiler_params=_params("arbitrary"),
    )(at, b)


def _inproj_bwd(du_a, du_b, du_g, w_a, w_b, w_g, x, dx2, g, exchange=None, tm=256):
    s, d = x.shape
    nb = s // tm

    def body(*refs):
        ((da_ref, db_ref, dg_ref, wa_ref, wb_ref, wg_ref, x_ref, dx2_ref, g_ref), (gx_ref, gg_ref), _,
         moves) = _split_refs(refs, 9, 2, exchange)
        i = pl.program_id(0)
        if moves:
            moves.start(also=(i == 0))

        @pl.when(i == 0)
        def _():
            gg_ref[...] = jnp.zeros_like(gg_ref)

        dh = _dot_nt(da_ref[...], wa_ref[...])
        dh += _dot_nt(db_ref[...], wb_ref[...])
        dh += _dot_nt(dg_ref[...], wg_ref[...])
        xv = x_ref[...]
        r = lax.rsqrt(jnp.mean(xv * xv, axis=-1, keepdims=True) + RMS_EPS)
        xh = xv * r
        gg_ref[...] += _colsum(dh * xh)
        dxh = dh * g_ref[...]
        gx_ref[...] = dx2_ref[...] + r * (dxh - xh * jnp.mean(dxh * xh, axis=-1, keepdims=True))
        if moves:
            moves.wait(also=(i == nb - 1))

    row = lambda w: pl.BlockSpec((tm, w), lambda i: (i, 0))
    full = lambda a: pl.BlockSpec(a.shape, lambda i: (0, 0))
    ex_in = exchange.operands if exchange else []
    ex_out = exchange.out_shapes if exchange else []
    res = pl.pallas_call(
        body, name="inproj_bwd", grid=(nb,),
        in_specs=[row(SEC), row(SEC), row(GATE_COLS), full(w_a), full(w_b), full(w_g), row(d), row(d), full(g)]
                 + [ANY] * len(ex_in),
        out_specs=[row(d), pl.BlockSpec((1, d), lambda i: (0, 0))] + [ANY] * len(ex_out),
        out_shape=[jax.ShapeDtypeStruct((s, d), F32), jax.ShapeDtypeStruct((1, d), F32)] + ex_out,
        scratch_shapes=exchange.scratch() if exchange else [],
        compiler_params=_params("arbitrary"),
    )(du_a, du_b, du_g, w_a, w_b, w_g, x, dx2, g, *ex_in)
    return res[0], res[1], list(res[2:])


def _rwkv_elementwise(ua, prev_row, first, mu, wl, w0, a0, kkw, kaw, bd):
    tm = ua.shape[0]
    rows = lax.broadcasted_iota(jnp.int32, (tm, 1), 0)
    prev = jnp.where(first, jnp.zeros_like(prev_row), prev_row)
    shifted = jnp.where(rows == 0, prev, pltpu.roll(ua, 1, 0))
    delta = shifted - ua
    us = ua + delta * mu
    r = us[:, 0:512]
    k0 = us[:, 512:1024]
    v = us[:, 1024:1536]
    lo = us[:, 1536:1664]
    gate = us[:, 1664:2176]
    lane = lax.broadcasted_iota(jnp.int32, (1, LANES), 1)
    th = jnp.tanh(lo)
    lin = jnp.where(lane < LORA, th, lo)
    ll = jnp.dot(lin.astype(BF16), wl, preferred_element_type=F32)
    sz = _sigmoid(w0 + ll[:, :512])
    e = sz * math.exp(-0.5)
    dec = jnp.exp(-e)
    a = _sigmoid(a0 + ll[:, 512:])
    kk0 = k0 * kkw
    ss = _head_sum(kk0 * kk0, bd)
    nrm = jnp.maximum(jnp.sqrt(ss), 1e-12)
    kk = kk0 / nrm
    k = k0 * (1.0 + (a - 1.0) * kaw)
    return dict(delta=delta, us=us, r=r, k0=k0, v=v, lo=lo, gate=gate, th=th, lin=lin, sz=sz, e=e, dec=dec,
                a=a, kk0=kk0, ss=ss, nrm=nrm, kk=kk, k=k)


def _rwkv_front(x, g, w_a, mu, wl, w0, a0, kkw, kaw, tm=256):
    s, d = x.shape

    def body(x_ref, g_ref, wa_ref, mu_ref, wl_ref, w0_ref, a0_ref, kkw_ref, kaw_ref,
             h_ref, ua_ref, r_ref, w_ref, k_ref, v_ref, a_ref, b_ref, gate_ref, last_row):
        i = pl.program_id(0)
        xv = x_ref[...]
        h = (xv * lax.rsqrt(jnp.mean(xv * xv, axis=-1, keepdims=True) + RMS_EPS) * g_ref[...]).astype(BF16)
        h_ref[...] = h
        ua = jnp.dot(h, wa_ref[...], preferred_element_type=F32)
        ua_ref[...] = ua

        @pl.when(i == 0)
        def _():
            last_row[...] = jnp.zeros_like(last_row)

        f = _rwkv_elementwise(ua, last_row[...], i == 0, mu_ref[...], wl_ref[...], w0_ref[...],
                              a0_ref[...], kkw_ref[...], kaw_ref[...], _head_ones())
        last_row[...] = ua[tm - 1:tm, :]
        r_ref[...] = f["r"]
        w_ref[...] = f["dec"]
        k_ref[...] = f["k"]
        v_ref[...] = f["v"]
        a_ref[...] = -f["kk"]
        b_ref[...] = f["kk"] * f["a"]
        gate_ref[...] = f["gate"]

    vec = lambda w: pl.BlockSpec((1, w), lambda i: (0, 0))
    row = lambda w: pl.BlockSpec((tm, w), lambda i: (i, 0))
    return pl.pallas_call(
        body, name="rwkv_front", grid=(s // tm,),
        in_specs=[row(d), vec(d), pl.BlockSpec(w_a.shape, lambda i: (0, 0), pipeline_mode=pl.Buffered(1)),
                  vec(SEC), pl.BlockSpec((LANES, 2 * D_HALF), lambda i: (0, 0)),
                  vec(D_HALF), vec(D_HALF), vec(D_HALF), vec(D_HALF)],
        out_specs=[row(d), row(SEC)] + [row(D_HALF)] * 7,
        out_shape=[jax.ShapeDtypeStruct((s, d), BF16), jax.ShapeDtypeStruct((s, SEC), F32)]
                  + [jax.ShapeDtypeStruct((s, D_HALF), F32)] * 7,
        scratch_shapes=[pltpu.VMEM((1, SEC), F32)],
        compiler_params=_params("arbitrary"),
    )(x, g, w_a, mu, wl, w0, a0, kkw, kaw)


SCAN_TB = 128
N_PAIRS = 4


def _pair_sum(x, left):
    s_l = jnp.sum(jnp.where(left, x, 0.0), axis=1, keepdims=True)
    s_r = jnp.sum(jnp.where(left, 0.0, x), axis=1, keepdims=True)
    return jnp.where(left, s_l, s_r)


def _pair_dot(x, row_l, row_r, left):
    s_l = jnp.sum(x * row_l, axis=1, keepdims=True)
    s_r = jnp.sum(x * row_r, axis=1, keepdims=True)
    return jnp.where(left, s_l, s_r)


def _halves(rows8):
    lane = lax.broadcasted_iota(jnp.int32, rows8.shape, 1)
    keep_left = (lane & (LANES - 1)) < HEAD
    return jnp.where(keep_left, rows8, 0.0), jnp.where(keep_left, 0.0, rows8)


def _quad_consts():
    lane = lax.broadcasted_iota(jnp.int32, (HEAD, 2 * LANES), 1)
    rowi = lax.broadcasted_iota(jnp.int32, (HEAD, 2 * LANES), 0)
    diag2 = rowi == (lane & (HEAD - 1))
    r = lax.broadcasted_iota(jnp.int32, (2 * LANES, 2 * LANES), 0) >> 6
    c = lax.broadcasted_iota(jnp.int32, (2 * LANES, 2 * LANES), 1) >> 6
    return diag2, (r == c).astype(BF16)


def _rows_to_columns(x8, diag2, bd2):
    lhs = jnp.concatenate([jnp.where(diag2, x8[i:i + 1], 0.0).astype(BF16) for i in range(SUBLANES)], axis=0)
    return jnp.dot(lhs, bd2, preferred_element_type=F32)


def _diag_rows(qtile, diag2, bd2, sub_row2):
    res = jnp.dot(qtile, bd2, preferred_element_type=F32)
    out = jnp.zeros((SUBLANES, 2 * LANES), F32)
    for i in range(SUBLANES):
        out = jnp.where(sub_row2 == i, _colsum(jnp.where(diag2, res[i * HEAD:(i + 1) * HEAD], 0.0)), out)
    return out


def _store_tile(qbuf, slot, p, i, x):
    qbuf[slot, p // 2, i * HEAD:(i + 1) * HEAD, (p % 2) * LANES:(p % 2 + 1) * LANES] = x.astype(BF16)


def _left_half():
    return lax.broadcasted_iota(jnp.int32, (HEAD, LANES), 1) < HEAD


def _split_refs(refs, n_rows, n_out, exchange):
    n_in = len(exchange.operands) if exchange else 0
    n_ex_out = len(exchange.out_shapes) if exchange else 0
    refs = list(refs)
    rows, refs = refs[:n_rows], refs[n_rows:]
    ex_in, refs = refs[:n_in], refs[n_in:]
    outs, refs = refs[:n_out], refs[n_out:]
    ex_out, refs = refs[:n_ex_out], refs[n_ex_out:]
    scratch, sems = (refs[:-3], refs[-3:]) if exchange else (refs, None)
    moves = exchange.moves(ex_in, ex_out, sems) if exchange else None
    return rows, outs, scratch, moves


def _wkv_fwd(r, w, k, a, b, v, exchange=None):
    s = r.shape[0]
    tb = SCAN_TB
    nb = s // tb

    def body(*refs):
        (r_ref, w_ref, k_ref, a_ref, b_ref, v_ref), (y_ref, st_ref), (state, vbuf, qbuf), moves = _split_refs(
            refs, 6, 2, exchange)
        g = pl.program_id(0)
        if moves:
            moves.start(also=(g == 0))

        @pl.when(g == 0)
        def _():
            state[...] = jnp.zeros_like(state)
            qbuf[...] = jnp.zeros_like(qbuf)

        left = _left_half()
        diag2, bd2 = _quad_consts()
        sub_row2 = lax.broadcasted_iota(jnp.int32, (SUBLANES, 2 * LANES), 0)
        groups = tb // SUBLANES
        quads = [slice(g2 * 2 * LANES, (g2 + 1) * 2 * LANES) for g2 in range(2)]

        def rows_of(q):
            return pl.ds(pl.multiple_of(q * SUBLANES, SUBLANES), SUBLANES)

        def v_tiles(q, slot):
            v8 = v_ref[rows_of(q), :]
            for g2 in range(2):
                vbuf[slot, g2] = _rows_to_columns(v8[:, quads[g2]], diag2, bd2)

        def chain(q, slot):
            rows8 = rows_of(q)
            a8, w8, b8, k8, r8 = (x[rows8, :] for x in (a_ref, w_ref, b_ref, k_ref, r_ref))
            pairs = [slice(p * LANES, (p + 1) * LANES) for p in range(N_PAIRS)]
            a_next = pltpu.roll(a8, SUBLANES - 1, 0)
            (a8_l, a8_r), (wa8_l, wa8_r) = _halves(a8), _halves(w8 * a_next)
            ba8 =jnp.concatenate([_pair_sum(b8[:, pr] * a_next[:, pr], left[0:SUBLANES]) for pr in pairs], axis=1)
            ka8 = jnp.concatenate([_pair_sum(k8[:, pr] * a_next[:, pr], left[0:SUBLANES]) for pr in pairs], axis=1)
            sp = [state[p] for p in range(N_PAIRS)]
            for i in range(0, SUBLANES, 2):
                r0, r1 = slice(i, i + 1), slice(i + 1, i + 2)
                sums = [(_pair_dot(sp[p], a8_l[r0, pairs[p]], a8_r[r0, pairs[p]], left),
                         _pair_dot(sp[p], wa8_l[r0, pairs[p]], wa8_r[r0, pairs[p]], left)) for p in range(N_PAIRS)]
                sa0, sa1 = [s[0] for s in sums], [s[1] for s in sums]
                for p in range(N_PAIRS):
                    pr = pairs[p]
                    inner = slice((p % 2) * LANES, (p % 2 + 1) * LANES)
                    vt0 = vbuf[slot, p // 2, i * HEAD:(i + 1) * HEAD, inner]
                    vt1 = vbuf[slot, p // 2, (i + 1) * HEAD:(i + 2) * HEAD, inner]
                    sa_next = sa1[p] + sa0[p] * ba8[r0, pr] + vt0 * ka8[r0, pr]
                    s1 = sp[p] * w8[r0, pr] + sa0[p] * b8[r0, pr] + vt0 * k8[r0, pr]
                    st_ref[q * SUBLANES + i, p] = s1
                    _store_tile(qbuf, slot, p, i, s1 * r8[r0, pr])
                    s2 = s1 * w8[r1, pr] + sa_next * b8[r1, pr] + vt1 * k8[r1, pr]
                    st_ref[q * SUBLANES + i + 1, p] = s2
                    _store_tile(qbuf, slot, p, i + 1, s2 * r8[r1, pr])
                    sp[p] = s2
            for p in range(N_PAIRS):
                state[p] = sp[p]

        def y_rows(q, slot):
            for g2 in range(2):
                y_ref[rows_of(q), quads[g2]] = _diag_rows(qbuf[slot, g2], diag2, bd2, sub_row2)

        v_tiles(0, 0)

        def two_groups(j, carry):
            q0 = 2 * j
            v_tiles(q0 + 1, 1)
            chain(q0, 0)
            y_rows(jnp.maximum(q0 - 1, 0), 1)
            v_tiles(jnp.minimum(q0 + 2, groups - 1), 0)
            chain(q0 + 1, 1)
            y_rows(q0, 0)
            return carry

        lax.fori_loop(0, groups // 2, two_groups, 0)
        y_rows(groups - 1, 1)
        if moves:
            moves.wait(also=(g == nb - 1))

    rows = pl.BlockSpec((tb, D_HALF), lambda g: (g, 0))
    ex_in = exchange.operands if exchange else []
    ex_out = exchange.out_shapes if exchange else []
    res = pl.pallas_call(
        body, name="wkv_fwd", grid=(nb,),
        in_specs=[rows] * 6 + [ANY] * len(ex_in),
        out_specs=[rows, pl.BlockSpec((tb, N_PAIRS, HEAD, LANES), lambda g: (g, 0, 0, 0))] + [ANY] * len(ex_out),
        out_shape=[jax.ShapeDtypeStruct((s, D_HALF), F32),
                   jax.ShapeDtypeStruct((s, N_PAIRS, HEAD, LANES), F32)] + ex_out,
        scratch_shapes=[pltpu.VMEM((N_PAIRS, HEAD, LANES), F32),
                        pltpu.VMEM((2, 2, SUBLANES * HEAD, 2 * LANES), F32),
                        pltpu.VMEM((2, 2, SUBLANES * HEAD, 2 * LANES), BF16)]
                       + (exchange.scratch() if exchange else []),
        compiler_params=_params("arbitrary"),
    )(r, w, k, a, b, v, *ex_in)
    return res[0], res[1], list(res[2:])


def _wkv_bwd(r, w, k, a, b, v, dy, st, exchange=None):
    s = r.shape[0]
    tb = SCAN_TB
    nb = s // tb

    def body(*refs):
        ((r_ref, w_ref, k_ref, a_ref, b_ref, v_ref, dy_ref, st_ref, before_ref),
         (dr_ref, dw_ref, dk_ref, dv_ref, da_ref, db_ref), (dstate, vbuf, qbuf, sbuf),
         moves) = _split_refs(refs, 9, 6, exchange)
        g = pl.program_id(0)
        first_block = g == nb - 1
        if moves:
            moves.start(also=(g == 0))

        @pl.when(g == 0)
        def _():
            dstate[...] = jnp.zeros_like(dstate)
            qbuf[...] = jnp.zeros_like(qbuf)

        left = _left_half()
        diag2, bd2 = _quad_consts()
        sub_row = lax.broadcasted_iota(jnp.int32, (SUBLANES, LANES), 0)
        sub_row2 = lax.broadcasted_iota(jnp.int32, (SUBLANES, 2 * LANES), 0)
        groups = tb // SUBLANES
        quads = [slice(g2 * 2 * LANES, (g2 + 1) * 2 * LANES) for g2 in range(2)]
        row_refs = (dr_ref, dw_ref, dk_ref, da_ref, db_ref)

        def rows_of(q):
            return pl.ds(pl.multiple_of(q * SUBLANES, SUBLANES), SUBLANES)

        def state_before(q, i, p):
            if i > 0:
                return st_ref[q * SUBLANES + i - 1, p]
            return jnp.where(q == 0, jnp.where(first_block, 0.0, before_ref[0, p]),
                             st_ref[jnp.maximum(q * SUBLANES - 1, 0), p])

        def column_tiles(q, slot):
            rows8 = rows_of(q)
            for kind, ref in enumerate((v_ref, dy_ref)):
                x8 = ref[rows8, :]
                for g2 in range(2):
                    vbuf[slot, kind, g2] = _rows_to_columns(x8[:, quads[g2]], diag2, bd2)
            a8 = a_ref[rows8, :]
            for i in range(SUBLANES):
                for p in range(N_PAIRS):
                    _store_tile(sbuf, 0, p, i, state_before(q, i, p) * a8[i:i + 1, p * LANES:(p + 1) * LANES])
            for g2 in range(2):
                vbuf[slot, 2, g2] = jnp.dot(sbuf[0, g2], bd2, preferred_element_type=F32)

        def chain(q, slot):
            rows8 = rows_of(q)
            a8, w8, b8, k8, r8 = (x[rows8, :] for x in (a_ref, w_ref, b_ref, k_ref, r_ref))
            b8_l, b8_r = _halves(b8)
            dsp = [dstate[p] for p in range(N_PAIRS)]
            outs = [[jnp.zeros((SUBLANES, LANES), F32) for _ in row_refs] for _ in range(N_PAIRS)]
            after = [st_ref[q * SUBLANES + SUBLANES - 1, p] for p in range(N_PAIRS)]
            for i in reversed(range(SUBLANES)):
                row = slice(i, i + 1)
                pl_ = [slice(p * LANES, (p + 1) * LANES) for p in range(N_PAIRS)]
                tile = [(p // 2, slice(i * HEAD, (i + 1) * HEAD), slice((p % 2) * LANES, (p % 2 + 1) * LANES))
                        for p in range(N_PAIRS)]
                sp = [state_before(q, i, p) for p in range(N_PAIRS)]
                dyt = [vbuf[(slot, 1) + tile[p]] for p in range(N_PAIRS)]
                ds = [dsp[p] + dyt[p] * r8[row, pl_[p]] for p in range(N_PAIRS)]
                dsa = [_pair_dot(ds[p], b8_l[row, pl_[p]], b8_r[row, pl_[p]], left) for p in range(N_PAIRS)]
                sa = [vbuf[(slot, 2) + tile[p]] for p in range(N_PAIRS)]
                for p in range(N_PAIRS):
                    ar, wr, br, kr = (x[row, pl_[p]] for x in (a8, w8, b8, k8))
                    vt = vbuf[(slot, 0) + tile[p]]
                    dsp[p] = ds[p] * wr + dsa[p] * ar
                    new = (_colsum(after[p] * dyt[p]), _colsum(ds[p] * sp[p]), _colsum(ds[p] * vt),
                           _colsum(sp[p] * dsa[p]), _colsum(ds[p] * sa[p]))
                    outs[p] = [jnp.where(sub_row == i, n, o) for n, o in zip(new, outs[p])]
                    _store_tile(qbuf, slot, p, i, ds[p] * kr)
                after = sp
            for p in range(N_PAIRS):
                dstate[p] = dsp[p]
                for ref, o in zip(row_refs, outs[p]):
                    ref[rows8, p * LANES:(p + 1) * LANES] = o

        def dv_rows(q, slot):
            for g2 in range(2):
                dv_ref[rows_of(q), quads[g2]] = _diag_rows(qbuf[slot, g2], diag2, bd2, sub_row2)

        column_tiles(groups - 1, 0)

        def two_groups(j, carry):
            q0 = groups - 1 - 2 * j
            column_tiles(q0 - 1, 1)
            chain(q0, 0)
            dv_rows(jnp.minimum(q0 + 1, groups - 1), 1)
            column_tiles(jnp.maximum(q0 - 2, 0), 0)
            chain(q0 - 1, 1)
            dv_rows(q0, 0)
            return carry

        lax.fori_loop(0, groups // 2, two_groups, 0)
        dv_rows(0, 1)
        if moves:
            moves.wait(also=(g == nb - 1))

    rows = pl.BlockSpec((tb, D_HALF), lambda g: (nb - 1 - g, 0))
    ex_in = exchange.operands if exchange else []
    ex_out = exchange.out_shapes if exchange else []
    res = pl.pallas_call(
        body, name="wkv_bwd", grid=(nb,),
        in_specs=[rows] * 7 + [pl.BlockSpec((tb, N_PAIRS, HEAD, LANES), lambda g: (nb - 1 - g, 0, 0, 0)),
                               pl.BlockSpec((1, N_PAIRS, HEAD, LANES),
                                            lambda g: (jnp.maximum((nb - 1 - g) * tb - 1, 0), 0, 0, 0))]
                 + [ANY] * len(ex_in),
        out_specs=[rows] * 6 + [ANY] * len(ex_out),
        out_shape=[jax.ShapeDtypeStruct((s, D_HALF), F32)] * 6 + ex_out,
        scratch_shapes=[pltpu.VMEM((N_PAIRS, HEAD, LANES), F32),
                        pltpu.VMEM((2, 3, 2, SUBLANES * HEAD, 2 * LANES), F32),
                        pltpu.VMEM((2, 2, SUBLANES * HEAD, 2 * LANES), BF16),
                        pltpu.VMEM((1, 2, SUBLANES * HEAD, 2 * LANES), BF16)]
                       + (exchange.scratch() if exchange else []),
        compiler_params=_params("arbitrary"),
    )(r, w, k, a, b, v, dy, st, st, *ex_in)
    return list(res[:6]), list(res[6:])


def _rwkv_post_math(y, r, k, v, gate, lw, lb, rk, bd):
    mean = _head_sum(y, bd) * (1.0 / HEAD)
    yc = y - mean
    var = _head_sum(yc * yc, bd) * (1.0 / HEAD)
    rstd = lax.rsqrt(var + LNX_EPS)
    yn = yc * rstd
    rkk = _head_sum(r * k * rk, bd)
    sg = _sigmoid(gate)
    pre = yn * lw + lb + rkk * v
    return yn, rstd, rkk, sg, pre


def _rwkv_prep_bwd(u_a, h_t, grads, mu, wl, w0, a0, kkw, kaw, tm=256):
    s = u_a.shape[0]
    nb = s // tm
    d = h_t.shape[0]

    def body(ua_ref, prev_ref, ht_ref, drs_ref, dws_ref, dks_ref, dvs_ref, das_ref, dbs_ref, drb_ref, dkb_ref, dvb_ref,
             dgt_ref, mu_ref, wl_ref, w0_ref, a0_ref, kkw_ref, kaw_ref,
             du_ref, dwa_ref, dmu_ref, dwl_ref, dw0_ref, da0_ref, dkkw_ref, dkaw_ref, carry):
        i = pl.program_id(0)

        @pl.when(i == 0)
        def _():
            carry[...] = jnp.zeros_like(carry)
            for ref in (dwa_ref, dmu_ref, dwl_ref, dw0_ref, da0_ref, dkkw_ref, dkaw_ref):
                ref[...] = jnp.zeros_like(ref)

        bd = _head_ones()
        mu_v, wl_v, kkw_v, kaw_v = mu_ref[...], wl_ref[...], kkw_ref[...], kaw_ref[...]
        f = _rwkv_elementwise(ua_ref[...], prev_ref[7:8, :], i == nb - 1, mu_v, wl_v, w0_ref[...],
                              a0_ref[...], kkw_v, kaw_v, bd)
        a, kk, k0 = f["a"], f["kk"], f["k0"]
        dk = dks_ref[...] + dkb_ref[...]
        dbs = dbs_ref[...]
        dkk = dbs * a - das_ref[...]
        da = dbs * kk + dk * k0 * kaw_v
        dk0 = dk * (1.0 + (a - 1.0) * kaw_v)
        dkaw_ref[...] += _colsum(dk * k0 * (a - 1.0))
        inv = 1.0 / f["nrm"]
        proj = _head_sum(dkk * kk, bd)
        dkk0 = jnp.where(f["ss"] > 1e-24, (dkk - kk * proj) * inv, dkk * inv)
        dk0 = dk0 + dkk0 * kkw_v
        dkkw_ref[...] += _colsum(dkk0 * k0)
        dza = da * a * (1.0 - a)
        da0_ref[...] += _colsum(dza)
        dz = -dws_ref[...] * f["dec"] * f["e"] * (1.0 - f["sz"])
        dw0_ref[...] += _colsum(dz)
        dll = jnp.concatenate([dz, dza], axis=1).astype(BF16)
        dwl_ref[...] += _dot_tn(f["lin"].astype(BF16), dll)
        dlin = _dot_nt(dll, wl_v)
        lane = lax.broadcasted_iota(jnp.int32, (1, LANES), 1)
        th = f["th"]
        dlo = jnp.where(lane < LORA, dlin * (1.0 - th * th), dlin)
        dus = jnp.concatenate([drs_ref[...] + drb_ref[...], dk0, dvs_ref[...] + dvb_ref[...], dlo, dgt_ref[...]],
                              axis=1)
        dmu_ref[...] += _colsum(dus * f["delta"])
        g1 = dus * mu_v
        rows = lax.broadcasted_iota(jnp.int32, (tm, 1), 0)
        up = jnp.where(rows == tm - 1, carry[...], pltpu.roll(g1, tm - 1, 0))
        dua = dus - g1 + up
        du = dua.astype(BF16)
        du_ref[...] = du
        dwa_ref[...] += jnp.dot(ht_ref[...], du, preferred_element_type=F32)
        carry[...] = g1[0:1, :]

    rev = lambda w: pl.BlockSpec((tm, w), lambda i: (nb - 1 - i, 0))
    vec = lambda w: pl.BlockSpec((1, w), lambda i: (0, 0))
    wl_spec = pl.BlockSpec((LANES, 2 * D_HALF), lambda i: (0, 0))
    return pl.pallas_call(
        body, name="rwkv_prep_bwd", grid=(nb,),
        in_specs=[rev(SEC), pl.BlockSpec((8, SEC), lambda i: (jnp.maximum((nb - 1 - i) * (tm // 8) - 1, 0), 0)),
                  pl.BlockSpec((d, tm), lambda i: (0, nb - 1 - i))]
                 + [rev(D_HALF)] * 10 + [vec(SEC), wl_spec] + [vec(D_HALF)] * 4,
        out_specs=[rev(SEC), pl.BlockSpec((d, SEC), lambda i: (0, 0)), vec(SEC), wl_spec] + [vec(D_HALF)] * 4,
        out_shape=[jax.ShapeDtypeStruct((s, SEC), BF16), jax.ShapeDtypeStruct((d, SEC), F32),
                   jax.ShapeDtypeStruct((1, SEC), F32),
                   jax.ShapeDtypeStruct((LANES, 2 * D_HALF), F32)] + [jax.ShapeDtypeStruct((1, D_HALF), F32)] * 4,
        scratch_shapes=[pltpu.VMEM((1, SEC), F32)],
        compiler_params=_params("arbitrary"),
    )(u_a, u_a, h_t, *grads, mu, wl, w0, a0, kkw, kaw)


def _tri(tm, lower):
    r = lax.broadcasted_iota(jnp.int32, (tm, tm), 0)
    c = lax.broadcasted_iota(jnp.int32, (tm, tm), 1)
    return ((r >= c) if lower else (r <= c)).astype(BF16)


def _head_rms(x, g, bd):
    rinv = lax.rsqrt(_head_sum(x * x, bd) * (1.0 / HEAD) + RMS_EPS)
    xh = x * rinv
    return xh, rinv, xh * g


def _fox_front(h, w_b, fb, qg, kg, tm=256):
    s, d = h.shape

    def body(h_ref, wb_ref, fb_ref, qg_ref, kg_ref, ub_ref, q_ref, k_ref, v_ref, cc_ref, cr_ref, carry):
        i = pl.program_id(0)

        @pl.when(i == 0)
        def _():
            carry[...] = jnp.zeros_like(carry)

        ub_ref[...] = jnp.dot(h_ref[...], wb_ref[...], preferred_element_type=F32)
        bd = _head_ones()
        _, _, qn = _head_rms(ub_ref[:, 0:512], qg_ref[...], bd)
        _, _, kn = _head_rms(ub_ref[:, 512:1024], kg_ref[...], bd)
        q_ref[...] = (qn * ATT_SCALE).astype(BF16)
        k_ref[...] = kn.astype(BF16)
        v_ref[...] = ub_ref[:, 1024:1536].astype(BF16)
        lane = lax.broadcasted_iota(jnp.int32, (1, LANES), 1)
        logf = jnp.where(lane < N_HEADS, _log_sigmoid(ub_ref[:, 2048:2176] + fb_ref[...]), 0.0)
        cum = _exact_dot(logf, _tri(tm, True), ones_first=True) + carry[...]
        for h in range(N_HEADS):
            cc_ref[h] = jnp.broadcast_to(cum[:, h:h + 1], (tm, LANES))
        cr_ref[...] = jnp.transpose(cum)[0:N_HEADS, :]
        carry[...] = cum[tm - 1:tm, :]

    blk = pl.BlockSpec((tm, D_HALF), lambda i: (i, 0))
    return pl.pallas_call(
        body, name="fox_front", grid=(s // tm,),
        in_specs=[pl.BlockSpec((tm, d), lambda i: (i, 0)),
                  pl.BlockSpec(w_b.shape, lambda i: (0, 0), pipeline_mode=pl.Buffered(1)),
                  pl.BlockSpec((1, LANES), lambda i: (0, 0)),
                  pl.BlockSpec((1, D_HALF), lambda i: (0, 0)), pl.BlockSpec((1, D_HALF), lambda i: (0, 0))],
        out_specs=[pl.BlockSpec((tm, SEC), lambda i: (i, 0)), blk, blk, blk,
                   pl.BlockSpec((N_HEADS, tm, LANES), lambda i: (0, i, 0)), pl.BlockSpec((N_HEADS, tm), lambda i: (0, i))],
        out_shape=[jax.ShapeDtypeStruct((s, SEC), F32)] + [jax.ShapeDtypeStruct((s, D_HALF), BF16)] * 3
                  + [jax.ShapeDtypeStruct((N_HEADS, s, LANES), F32), jax.ShapeDtypeStruct((N_HEADS, s), F32)],
        scratch_shapes=[pltpu.VMEM((1, LANES), F32)],
        compiler_params=_params("arbitrary"),
    )(h, w_b, fb, qg, kg)


ATT_T = 256


def _tiles(nblk, by_query):
    if by_query:
        pairs = [(i, j) for i in range(nblk) for j in range(i + 1)]
    else:
        pairs = [(i, j) for j in range(nblk) for i in range(j, nblk)]
    return (jnp.asarray([p[0] for p in pairs], jnp.int32), jnp.asarray([p[1] for p in pairs], jnp.int32))


def _attn_fwd(q, k, v, cc, cr):
    s = q.shape[0]
    t = ATT_T
    nblk = s // t

    def body(qi_ref, kj_ref, q_ref, k_ref, v_ref, cc_ref, cr_ref, o_ref, lse_ref, m_sc, l_sc, acc_sc):
        i = qi_ref[pl.program_id(0)]
        j = kj_ref[pl.program_id(0)]

        @pl.when(j == 0)
        def _():
            m_sc[...] = jnp.full_like(m_sc, NEG)
            l_sc[...] = jnp.zeros_like(l_sc)
            acc_sc[...] = jnp.zeros_like(acc_sc)

        def tile(on_diagonal):
            causal = _causal_tile(t) if on_diagonal else None
            left = lax.broadcasted_iota(jnp.int32, (1, LANES), 1) < HEAD
            for p in range(N_PAIRS):
                lanes = slice(p * LANES, (p + 1) * LANES)
                q2, k2, v2 = q_ref[:, lanes], k_ref[:, lanes], v_ref[:, lanes]
                acc2 = acc_sc[:, lanes]
                for e in range(2):
                    h = 2 * p + e
                    msk = left if e == 0 else jnp.logical_not(left)
                    sc = _dot_nt(jnp.where(msk, q2, jnp.zeros_like(q2)), k2)
                    sc = sc + (_wide(cc_ref[h]) - cr_ref[h:h + 1, :])
                    if on_diagonal:
                        sc = jnp.where(causal, sc, NEG)
                    m_prev = m_sc[h]
                    m_new = jnp.maximum(m_prev, jnp.max(sc, axis=1, keepdims=True))
                    alpha = jnp.exp(m_prev - m_new)
                    pm = jnp.exp(sc - _wide(m_new))
                    l_sc[h] = alpha * l_sc[h] + jnp.sum(pm, axis=1, keepdims=True)
                    m_sc[h] = m_new
                    pv = jnp.dot(pm.astype(BF16), v2, preferred_element_type=F32)
                    acc2 = jnp.where(msk, alpha * acc2 + pv, acc2)
                acc_sc[:, lanes] = acc2

        pl.when(j < i)(functools.partial(tile, False))
        pl.when(j == i)(functools.partial(tile, True))

        @pl.when(j == i)
        def _():
            left = lax.broadcasted_iota(jnp.int32, (1, LANES), 1) < HEAD
            for p in range(N_PAIRS):
                lanes = slice(p * LANES, (p + 1) * LANES)
                inv = jnp.where(left, 1.0 / l_sc[2 * p], 1.0 / l_sc[2 * p + 1])
                o_ref[:, lanes] = acc_sc[:, lanes] * inv
            for h in range(N_HEADS):
                lse_ref[h] = m_sc[h] + jnp.log(l_sc[h])

    qi, kj = _tiles(nblk, by_query=True)
    qblk = pl.BlockSpec((t, D_HALF), lambda n, qi, kj: (qi[n], 0))
    kblk = pl.BlockSpec((t, D_HALF), lambda n, qi, kj: (kj[n], 0))
    qrep = pl.BlockSpec((N_HEADS, t, LANES), lambda n, qi, kj: (0, qi[n], 0))
    return pl.pallas_call(
        body, name="fox_attn_fwd",
        grid_spec=pltpu.PrefetchScalarGridSpec(
            num_scalar_prefetch=2, grid=(qi.shape[0],),
            in_specs=[qblk, kblk, kblk, qrep, pl.BlockSpec((N_HEADS, t), lambda n, qi, kj: (0, kj[n]))],
            out_specs=[qblk, qrep],
            scratch_shapes=[pltpu.VMEM((N_HEADS, t, LANES), F32), pltpu.VMEM((N_HEADS, t, LANES), F32),
                            pltpu.VMEM((t, D_HALF), F32)]),
        out_shape=[jax.ShapeDtypeStruct((s, D_HALF), F32), jax.ShapeDtypeStruct((N_HEADS, s, LANES), F32)],
        compiler_params=_params("arbitrary"),
    )(qi, kj, q, k, v, cc, cr)


def _causal_tile(t):
    return lax.broadcasted_iota(jnp.int32, (t, t), 0) >= lax.broadcasted_iota(jnp.int32, (t, t), 1)


def _wide(x):
    return jnp.concatenate([x, x], axis=1)


def _attn_probs(q2, k2, v2, do2, msk, causal, bias, lse_rows):
    zero = jnp.zeros_like(q2)
    qh = jnp.where(msk, q2, zero)
    doh = jnp.where(msk, do2, zero)
    sc = _dot_nt(qh, k2) + bias
    if causal is not None:
        sc = jnp.where(causal, sc, NEG)
    pm = jnp.exp(sc - _wide(lse_rows))
    dp = _dot_nt(doh, v2)
    return qh, doh, pm, dp


def _attn_bwd_rowdot(q, k, v, do, lse, cc, cr):
    s = q.shape[0]
    t = ATT_T
    nblk = s // t

    def body(qi_ref, kj_ref, q_ref, k_ref, v_ref, do_ref, lse_ref, cc_ref, cr_ref, dd_ref, acc):
        i = qi_ref[pl.program_id(0)]
        j = kj_ref[pl.program_id(0)]

        @pl.when(j == 0)
        def _():
            acc[...] = jnp.zeros_like(acc)

        def tile(on_diagonal):
            causal = _causal_tile(t) if on_diagonal else None
            left = lax.broadcasted_iota(jnp.int32, (1, LANES), 1) < HEAD
            for p in range(N_PAIRS):
                lanes = slice(p * LANES, (p + 1) * LANES)
                q2, k2, v2, do2 = q_ref[:, lanes], k_ref[:, lanes], v_ref[:, lanes], do_ref[:, lanes]
                for e in range(2):
                    h = 2 * p + e
                    msk = left if e == 0 else jnp.logical_not(left)
                    bias = _wide(cc_ref[h]) - cr_ref[h:h + 1, :]
                    _, _, pm, dp = _attn_probs(q2, k2, v2, do2, msk, causal, bias, lse_ref[h])
                    acc[h] += jnp.sum(pm * dp, axis=1, keepdims=True)

        pl.when(j < i)(functools.partial(tile, False))
        pl.when(j == i)(functools.partial(tile, True))

        @pl.when(j == i)
        def _():
            dd_ref[...] = acc[...]

    qi, kj = _tiles(nblk, by_query=True)
    qblk = pl.BlockSpec((t, D_HALF), lambda n, qi, kj: (qi[n], 0))
    qcol = pl.BlockSpec((N_HEADS, t, LANES), lambda n, qi, kj: (0, qi[n], 0))
    kblk = pl.BlockSpec((t, D_HALF), lambda n, qi, kj: (kj[n], 0))
    return pl.pallas_call(
        body, name="fox_attn_rowdot",
        grid_spec=pltpu.PrefetchScalarGridSpec(
            num_scalar_prefetch=2, grid=(qi.shape[0],),
            in_specs=[qblk, kblk, kblk, qblk, qcol, qcol, pl.BlockSpec((N_HEADS, t), lambda n, qi, kj: (0, kj[n]))],
            out_specs=qcol, scratch_shapes=[pltpu.VMEM((N_HEADS, t, LANES), F32)]),
        out_shape=jax.ShapeDtypeStruct((N_HEADS, s, LANES), F32),
        compiler_params=_params("arbitrary"),
    )(qi, kj, q, k, v, do, lse, cc, cr)


def _attn_bwd(q, k, v, do, lse, dd, cc, cr):
    s = q.shape[0]
    t = ATT_T
    nblk = s // t

    def body(qi_ref, kj_ref, q_ref, k_ref, v_ref, do_ref, lse_ref, dd_ref, cc_ref, cr_ref,
             dq_ref, dk_ref, dv_ref, dcr_ref, dk_sc, dv_sc, dcr_sc):
        i = qi_ref[pl.program_id(0)]
        j = kj_ref[pl.program_id(0)]

        @pl.when(pl.program_id(0) == 0)
        def _():
            dq_ref[...] = jnp.zeros_like(dq_ref)

        @pl.when(i == j)
        def _():
            dk_sc[...] = jnp.zeros_like(dk_sc)
            dv_sc[...] = jnp.zeros_like(dv_sc)
            dcr_sc[...] = jnp.zeros_like(dcr_sc)

        def tile(on_diagonal):
            causal = _causal_tile(t) if on_diagonal else None
            left = lax.broadcasted_iota(jnp.int32, (1, LANES), 1) < HEAD
            qrows = pl.ds(pl.multiple_of(i * t, t), t)
            for p in range(N_PAIRS):
                lanes = slice(p * LANES, (p + 1) * LANES)
                q2, k2, v2, do2 = q_ref[:, lanes], k_ref[:, lanes], v_ref[:, lanes], do_ref[:, lanes]
                zero = jnp.zeros_like(q2)
                dq2 = jnp.zeros((t, LANES), F32)
                dk2 = jnp.zeros((t, LANES), F32)
                dv2 = jnp.zeros((t, LANES), F32)
                for e in range(2):
                    h = 2 * p + e
                    msk = left if e == 0 else jnp.logical_not(left)
                    bias = _wide(cc_ref[h]) - cr_ref[h:h + 1, :]
                    qh, doh, pm, dp = _attn_probs(q2, k2, v2, do2, msk, causal, bias, lse_ref[h])
                    dsc = pm * (dp - _wide(dd_ref[h]))
                    dsb = dsc.astype(BF16)
                    dv2 += _dot_tn(pm.astype(BF16), doh)
                    dk2 += _dot_tn(dsb, qh)
                    dq2 += jnp.dot(dsb, jnp.where(msk, k2, zero), preferred_element_type=F32)
                    dcr_sc[h:h + 1, :] += -_colsum(dsc)
                dq_ref[qrows, lanes] += dq2 * ATT_SCALE
                dk_sc[:, lanes] += dk2
                dv_sc[:, lanes] += dv2

        pl.when(i > j)(functools.partial(tile, False))
        pl.when(i == j)(functools.partial(tile, True))

        @pl.when(i == nblk - 1)
        def _():
            dk_ref[...] = dk_sc[...]
            dv_ref[...] = dv_sc[...]
            dcr_ref[...] = dcr_sc[...]

    qi, kj = _tiles(nblk, by_query=False)
    qblk = pl.BlockSpec((t, D_HALF), lambda n, qi, kj: (qi[n], 0))
    qcol = pl.BlockSpec((N_HEADS, t, LANES), lambda n, qi, kj: (0, qi[n], 0))
    kblk = pl.BlockSpec((t, D_HALF), lambda n, qi, kj: (kj[n], 0))
    krow = pl.BlockSpec((N_HEADS, t), lambda n, qi, kj: (0, kj[n]))
    return pl.pallas_call(
        body, name="fox_attn_bwd",
        grid_spec=pltpu.PrefetchScalarGridSpec(
            num_scalar_prefetch=2, grid=(qi.shape[0],),
            in_specs=[qblk, kblk, kblk, qblk, qcol, qcol, qcol, krow],
            out_specs=[pl.BlockSpec((s, D_HALF), lambda n, qi, kj: (0, 0)), kblk, kblk, krow],
            scratch_shapes=[pltpu.VMEM((t, D_HALF), F32), pltpu.VMEM((t, D_HALF), F32), pltpu.VMEM((N_HEADS, t), F32)]),
        out_shape=[jax.ShapeDtypeStruct((s, D_HALF), F32)] * 3 + [jax.ShapeDtypeStruct((N_HEADS, s), F32)],
        compiler_params=_params("arbitrary"),
    )(qi, kj, q, k, v, do, lse, dd, cc, cr)


def _fox_prep_bwd(u_b, h_t, dq, dk, dv, dgate, dcum, fb, qg, kg, tm=256):
    s = u_b.shape[0]
    nb = s // tm
    d = h_t.shape[0]

    def body(ub_ref, ht_ref, dq_ref, dk_ref, dv_ref, dg_ref, dc_ref, fb_ref, qg_ref, kg_ref,
             du_ref, dwb_ref, dqg_ref, dkg_ref, dfb_ref, carry):
        i = pl.program_id(0)

        @pl.when(i == 0)
        def _():
            carry[...] = jnp.zeros_like(carry)
            dwb_ref[...] = jnp.zeros_like(dwb_ref)
            dqg_ref[...] = jnp.zeros_like(dqg_ref)
            dkg_ref[...] = jnp.zeros_like(dkg_ref)
            dfb_ref[...] = jnp.zeros_like(dfb_ref)

        bd = _head_ones()
        for lo, g_ref, d_ref, dgain_ref in ((0, qg_ref, dq_ref, dqg_ref), (512, kg_ref, dk_ref, dkg_ref)):
            gain = g_ref[...]
            xh, rinv, _ = _head_rms(ub_ref[:, lo:lo + 512], gain, bd)
            dn = d_ref[...]
            dgain_ref[...] += _colsum(dn * xh)
            dxh = dn * gain
            du_ref[:, lo:lo + 512] = (rinv * (dxh - xh * (_head_sum(dxh * xh, bd) * (1.0 / HEAD)))).astype(BF16)
        du_ref[:, 1024:1536] = dv_ref[...].astype(BF16)
        du_ref[:, 1536:2048] = dg_ref[...].astype(BF16)
        lane = lax.broadcasted_iota(jnp.int32, (1, LANES), 1)
        dc = dc_ref[...]
        dlogf = _exact_dot(dc, _tri(tm, False), ones_first=True) + carry[...]
        carry[...] += _colsum(dc)
        fl = ub_ref[:, 2048:2176] + fb_ref[...]
        dfl = jnp.where(lane < N_HEADS, dlogf * (1.0 - _sigmoid(fl)), 0.0)
        du_ref[:, 2048:2176] = dfl.astype(BF16)
        dfb_ref[...] += _colsum(dfl)
        dwb_ref[...] += jnp.dot(ht_ref[...], du_ref[...], preferred_element_type=F32)

    rev = lambda w: pl.BlockSpec((tm, w), lambda i: (nb - 1 - i, 0))
    vec = lambda w: pl.BlockSpec((1, w), lambda i: (0, 0))
    return pl.pallas_call(
        body, name="fox_prep_bwd", grid=(nb,),
        in_specs=[rev(SEC), pl.BlockSpec((d, tm), lambda i: (0, nb - 1 - i))] + [rev(D_HALF)] * 4
                 + [rev(LANES), vec(LANES), vec(D_HALF), vec(D_HALF)],
        out_specs=[rev(SEC), pl.BlockSpec((d, SEC), lambda i: (0, 0)), vec(D_HALF), vec(D_HALF), vec(LANES)],
        out_shape=[jax.ShapeDtypeStruct((s, SEC), BF16), jax.ShapeDtypeStruct((d, SEC), F32),
                   jax.ShapeDtypeStruct((1, D_HALF), F32), jax.ShapeDtypeStruct((1, D_HALF), F32),
                   jax.ShapeDtypeStruct((1, LANES), F32)],
        scratch_shapes=[pltpu.VMEM((1, LANES), F32)],
        compiler_params=_params("arbitrary"),
    )(u_b, h_t, dq, dk, dv, dgate, dcum, fb, qg, kg)


def _merge(y, r, k, v, gate_a, o, u_b, h, x, tgt, w_g, wa, wb, wo, fg, lw, lb, rk, tm=256):
    s, d = x.shape

    def body(y_ref, r_ref, k_ref, v_ref, ga_ref, o_ref, gb_ref, h_ref, x_ref, t_ref, wg_ref, wa_ref, wb_ref, wo_ref,
             fg_ref, lw_ref, lb_ref, rk_ref,
             dx2_ref, dy_ref, drb_ref, dkb_ref, dvb_ref, dga_ref, do_ref, dgb_ref, dug_ref,
             dwa_ref, dwb_ref, dwo_ref, dfg_ref, loss_ref, dlw_ref, dlb_ref, drk_ref):
        i = pl.program_id(0)

        @pl.when(i == 0)
        def _():
            for ref in (dwa_ref, dwb_ref, dwo_ref, dfg_ref, loss_ref, dlw_ref, dlb_ref, drk_ref):
                ref[...] = jnp.zeros_like(ref)

        bd = _head_ones()
        wa_v, wb_v, wo_v, fg_v = wa_ref[...], wb_ref[...], wo_ref[...], fg_ref[...]
        rv, kv, vv, ga, lw_v, rk_v = r_ref[...], k_ref[...], v_ref[...], ga_ref[...], lw_ref[...], rk_ref[...]
        yn, rstd, rkk, sga, pre = _rwkv_post_math(y_ref[...], rv, kv, vv, ga, lw_v, lb_ref[...], rk_v, bd)
        silu_a = ga * sga
        gb, ov = gb_ref[...], o_ref[...]
        sgb = _sigmoid(gb)
        silu_b = gb * sgb
        ma = (pre * silu_a).astype(BF16)
        mb = (ov * silu_b).astype(BF16)
        ya = jnp.dot(ma, wa_v, preferred_element_type=F32)
        yb = jnp.dot(mb, wb_v, preferred_element_type=F32)
        ug = jnp.dot(h_ref[...], wg_ref[...], preferred_element_type=F32)
        sa = _sigmoid(ug[:, 0:d])
        sb = _sigmoid(ug[:, d:2 * d])
        merged = (sa * ya + sb * yb).astype(BF16)
        x2 = x_ref[...] + jnp.dot(merged, wo_v, preferred_element_type=F32)
        r2 = lax.rsqrt(jnp.mean(x2 * x2, axis=-1, keepdims=True) + RMS_EPS)
        x2h = x2 * r2
        err = x2h * fg_v - t_ref[...]
        loss_ref[...] += _colsum(err * err)
        dyo = err * (1.0 / d)
        dfg_ref[...] += _colsum(dyo * x2h)
        dx2h = dyo * fg_v
        dx2 = r2 * (dx2h - x2h * jnp.mean(dx2h * x2h, axis=-1, keepdims=True))
        dx2_ref[...] = dx2
        dx2b = dx2.astype(BF16)
        dmerged = _dot_nt(dx2b, wo_v)
        dwo_ref[...] += _dot_tn(merged, dx2b)
        dya = dmerged * sa
        dyb = dmerged * sb
        dug_ref[:, 0:d] = (dya * ya * (1.0 - sa)).astype(BF16)
        dug_ref[:, d:2 * d] = (dyb * yb * (1.0 - sb)).astype(BF16)
        dyab = dya.astype(BF16)
        dybb = dyb.astype(BF16)
        dwa_ref[...] += _dot_tn(ma, dyab)
        dwb_ref[...] += _dot_tn(mb, dybb)
        dmb = _dot_nt(dybb, wb_v)
        do_ref[...] = (dmb * silu_b).astype(BF16)
        dgb_ref[...] = dmb * ov * (sgb * (1.0 + gb * (1.0 - sgb)))
        dma = _dot_nt(dyab, wa_v)
        dga_ref[...] = dma * pre * (sga * (1.0 + ga * (1.0 - sga)))
        dpre = dma * silu_a
        dlw_ref[...] += _colsum(dpre * yn)
        dlb_ref[...] += _colsum(dpre)
        dyn = dpre * lw_v
        m1 = _head_sum(dyn, bd) * (1.0 / HEAD)
        m2 = _head_sum(dyn * yn, bd) * (1.0 / HEAD)
        dy_ref[...] = rstd * (dyn - m1 - yn * m2)
        dvb_ref[...] = dpre * rkk
        drkk = _head_sum(dpre * vv, bd)
        drb_ref[...] = drkk * kv * rk_v
        dkb_ref[...] = drkk * rv * rk_v
        drk_ref[...] += _colsum(drkk * rv * kv)

    row = lambda w: pl.BlockSpec((tm, w), lambda i: (i, 0))
    full = lambda a: pl.BlockSpec(a.shape, lambda i: (0, 0))
    once = lambda a: pl.BlockSpec(a.shape, lambda i: (0, 0), pipeline_mode=pl.Buffered(1))
    half = jax.ShapeDtypeStruct((s, D_HALF), F32)
    fshape = lambda a: jax.ShapeDtypeStruct(a.shape, F32)
    return pl.pallas_call(
        body, name="merge_fwd_bwd", grid=(s // tm,),
        in_specs=[row(D_HALF)] * 6 + [pl.BlockSpec((tm, D_HALF), lambda i: (i, 3)), row(d), row(d), row(d),
                                      once(w_g), once(wa), once(wb), once(wo), full(fg), full(lw), full(lb), full(rk)],
        out_specs=[row(d)] + [row(D_HALF)] * 7 + [row(GATE_COLS), full(wa), full(wb), full(wo), full(fg), full(fg),
                                                   full(lw), full(lb), full(rk)],
        out_shape=[jax.ShapeDtypeStruct((s, d), F32)] + [half] * 5 + [jax.ShapeDtypeStruct((s, D_HALF), BF16), half,
                                                                    jax.ShapeDtypeStruct((s, GATE_COLS), BF16),
                                                                    fshape(wa), fshape(wb), fshape(wo), fshape(fg),
                                                                    fshape(fg), fshape(lw), fshape(lb), fshape(rk)],
        compiler_params=_params("arbitrary"),
    )(y, r, k, v, gate_a, o, u_b, h, x, tgt, w_g, wa, wb, wo, fg, lw, lb, rk)


def _lora_weight(w_up, a_up):
    z = jnp.zeros((LORA, D_HALF), w_up.dtype)
    return jnp.concatenate([jnp.concatenate([w_up, z], axis=1), jnp.concatenate([z, a_up], axis=1)], axis=0)


def _device_grads(x, tgt, p, w_a, w_up, a_up, late_weights, fwd_exchange=None, bwd_exchange=None, tail_exchange=None):
    wl = _lora_weight(w_up, a_up)
    rk = p["r_k"].reshape(1, D_HALF)
    fb = jnp.pad(p["f_bias"], ((0, 0), (0, LANES - N_HEADS)))
    qg = jnp.tile(p["q_norm_g"], (1, N_HEADS))
    kg = jnp.tile(p["k_norm_g"], (1, N_HEADS))
    fg = p["final_norm_g"].reshape(1, D_MODEL)
    mixer = (p["shift_mu"], wl, p["w0"], p["a0"], p["k_k"], p["k_a"])

    h, u_a, r, dec, k, v, av, bv, gate_a = _rwkv_front(x, p["norm_g"], w_a, *mixer)
    y, st, arrived = _wkv_fwd(r, dec, k, av, bv, v, fwd_exchange)

    w_b, w_g, w_out_a, w_out_b, w_out = late_weights(arrived)
    u_b, q, kn, vb, cc, cr = _fox_front(h, w_b, fb, qg, kg)
    o, lse = _attn_fwd(q, kn, vb, cc, cr)

    (dx2, dy, dr_b, dk_b, dv_b, dgate_a, do, dgate_b, du_g, dwa, dwb, dwo, dfg, loss_vec, dlw, dlb, drk) = _merge(
        y, r, k, v, gate_a, o, u_b, h, x, tgt, w_g, w_out_a, w_out_b, w_out, fg, p["lnx_w"], p["lnx_b"], rk)

    dd = _attn_bwd_rowdot(q, kn, vb, do, lse, cc, cr)
    dq, dk_att, dv_att, dcr = _attn_bwd(q, kn, vb, do, lse, dd, cc, cr)
    dcum = jnp.pad(dcr.T, ((0, 0), (0, LANES - N_HEADS)))
    h_t = h.T
    du_b, dw_b, dqg, dkg, dfb = _fox_prep_bwd(u_b, h_t, dq, dk_att, dv_att, dgate_b, dcum, fb, qg, kg)
    dw_g = _matmul_tn_acc(h_t, du_g, "dw_gate")

    scan_grads, sent = _wkv_bwd(r, dec, k, av, bv, v, dy, st,
                                bwd_exchange(dw_b, dw_g, dwa, dwb, dwo) if bwd_exchange else None)
    du_a, dw_a, dmu, dwl, dw0, da0, dkkw, dkaw = _rwkv_prep_bwd(
        u_a, h_t, (*scan_grads, dr_b, dk_b, dv_b, dgate_a), *mixer)
    dw_up, da_up = dwl[:LORA, :D_HALF], dwl[LORA:, D_HALF:]
    sent_last = _run_on_sequencer(tail_exchange(dw_a, dw_up, da_up), "scatter_tail", 1) if tail_exchange else []
    grad_x, dnorm_g, _ = _inproj_bwd(du_a, du_b, du_g, w_a, w_b, w_g, x, dx2, p["norm_g"])

    grads = dict(
        norm_g=dnorm_g, w_in=(dw_a, dw_b, dw_g), shift_mu=dmu,
        w_lora_up=dw_up, w0=dw0, a_lora_up=da_up, a0=da0, k_k=dkkw, k_a=dkaw,
        r_k=drk.reshape(1, N_HEADS, HEAD), lnx_w=dlw, lnx_b=dlb, f_bias=dfb[:, :N_HEADS],
        q_norm_g=dqg.reshape(N_HEADS, HEAD).sum(axis=0, keepdims=True),
        k_norm_g=dkg.reshape(N_HEADS, HEAD).sum(axis=0, keepdims=True),
        w_out_a=dwa, w_out_b=dwb, w_out=dwo, final_norm_g=dfg.reshape(D_MODEL))
    return loss_vec, grad_x, grads, sent, sent_last


CHIP_FLIPS = ((1, 0), (0, 1), (1, 1))
ANY = pl.BlockSpec(memory_space=pl.ANY)


def _position():
    return lax.axis_index("x"), lax.axis_index("y"), lax.axis_index("c")


def _flip(v, f):
    return 1 - v if f else v


def _both(a, b):
    if a is None:
        return b
    return a if b is None else jnp.logical_and(a, b)


def _when(cond, fn):
    if cond is None:
        fn()
    else:
        pl.when(cond)(fn)


class _Moves:
    def __init__(self, send_sems, recv_sems, local_sems):
        self.send_sems, self.recv_sems, self.local_sems = send_sems, recv_sems, local_sems
        self.remote, self.local = [], []

    def send(self, src, dst, peer, landing, send_if=None, recv_if=None, first=False):
        k = len(self.remote)
        sems = dict(send_sem=self.send_sems.at[k], recv_sem=self.recv_sems.at[k], device_id=peer, device_id_type=MESH)
        out = pltpu.make_async_remote_copy(src_ref=src, dst_ref=dst, **sems)
        arrival = pltpu.make_async_remote_copy(src_ref=src, dst_ref=landing, **sems)
        self.remote.append((out, arrival, send_if, recv_if, first))

    def copy(self, src, dst, cond=None):
        cp = pltpu.make_async_copy(src, dst, self.local_sems.at[len(self.local)])
        self.local.append((cp, cond))

    def start(self, also=None):
        for cp, cond in self.local:
            _when(_both(also, cond), cp.start)
        for out, _, send_if, _, _ in self.remote:
            _when(_both(also, send_if), out.start)

    def wait_arrivals(self, also=None, first=None):
        for _, arrival, _, recv_if, is_first in self.remote:
            if first is None or first == is_first:
                _when(_both(also, recv_if), arrival.wait_recv)

    def wait_sent(self, also=None):
        for out, _, send_if, _, _ in self.remote:
            _when(_both(also, send_if), out.wait_send)
        for cp, cond in self.local:
            _when(_both(also, cond), cp.wait)

    def wait(self, also=None):
        self.wait_arrivals(also)
        self.wait_sent(also)


class _Exchange:
    def __init__(self, operands, out_shapes, n_remote, n_local, build, relays=None, in_place=(), n_staging=0):
        self.operands, self.out_shapes = list(operands), list(out_shapes)
        self.n_remote, self.n_local, self.build = n_remote, n_local, build
        self.relays, self.in_place = relays, in_place
        self.n_staging = n_staging

    def scratch(self):
        return [pltpu.SemaphoreType.DMA((self.n_remote,)), pltpu.SemaphoreType.DMA((self.n_remote,)),
                pltpu.SemaphoreType.DMA((max(self.n_local, 1),))]

    def moves(self, in_refs, out_refs, sems):
        mv = _Moves(*sems)
        self.build(mv, in_refs, out_refs)
        return mv


def _run_on_sequencer(exchange, name, collective_id):
    ins = [jax.new_ref(a, memory_space=pltpu.MemorySpace.HBM) for a in exchange.operands]
    outs = [ins[i] if i in exchange.in_place else jax.empty_ref(s, memory_space=pltpu.MemorySpace.HBM)
            for i, s in enumerate(exchange.out_shapes)]
    forward, to_sibling = exchange.relays or (None, None)
    relay_scratch = [pltpu.SemaphoreType.DMA((stage[0],)) for stage in (forward, to_sibling) if stage for _ in range(2)]

    def launch(*sems):
        x, y, c = _position()
        peers = [(_flip(x, fx), _flip(y, fy), c) for fx, fy in CHIP_FLIPS] + ([(x, y, 1 - c)] if to_sibling else [])
        barrier = pltpu.get_barrier_semaphore()
        for peer in peers:
            pl.semaphore_signal(barrier, inc=1, device_id=peer, device_id_type=MESH)
        pl.semaphore_wait(barrier, len(peers))
        moves = exchange.moves(ins, outs, sems[:3])
        moves.start()
        later = []
        if forward:
            onward = _Moves(sems[3], sems[4], None)
            forward[1](onward, ins, outs)
            moves.wait_arrivals(first=True)
            onward.start()
            moves.wait_arrivals(first=False)
            onward.wait_arrivals()
            later.append(onward)
        else:
            moves.wait_arrivals()
        if to_sibling:
            passed = _Moves(*sems[-2:], None)
            to_sibling[1](passed, ins, outs)
            passed.start()
            passed.wait_arrivals()
            later.append(passed)
        for mv in later + [moves]:
            mv.wait_sent()

    pl.kernel(launch, mesh=plsc.ScalarSubcoreMesh(axis_name="sequencer", num_cores=1), name=name,
              scratch_types=tuple(exchange.scratch() + relay_scratch),
              compiler_params=pltpu.CompilerParams(collective_id=collective_id))()
    return [o[...] for o in outs[:len(outs) - exchange.n_staging]]


def _row_major_copy(a, name):
    r, c = a.shape
    tr = _row_tile(r)

    def body(a_ref, o_ref):
        o_ref[...] = a_ref[...]

    blk = pl.BlockSpec((tr, c), lambda i: (i, 0))
    return pl.pallas_call(body, name=name, grid=(r // tr,), in_specs=[blk], out_specs=blk,
                          out_shape=jax.ShapeDtypeStruct(a.shape, a.dtype), compiler_params=_params("parallel"))(a)


def _is_chip(x, y, chip):
    return jnp.logical_and(x == chip // 2, y == chip % 2)


def _gather_exchange(from_chip, from_all, split=()):
    n1, n2 = len(from_chip), len(from_all)
    near = CHIP_FLIPS[:2]

    def quarters(t, c, first, count=1):
        n = from_chip[t][1].shape[0] // 4
        return pl.ds((2 * c + first) * n, count * n)

    def build(mv, ins, outs):
        x, y, c = _position()
        me = 2 * x + y
        for t, (chip, _) in enumerate(from_chip):
            if t not in split:
                mv.copy(ins[t], outs[t], cond=_is_chip(x, y, chip))
        for t in range(n2):
            mv.copy(ins[n1 + t], outs[n1 + t].at[me])
        for t in split:
            for first in (True, False):
                for f, (fx, fy) in enumerate(near):
                    px, py = _flip(x, fx), _flip(y, fy)
                    part = quarters(t, c, f if first else 1 - f)
                    mv.send(ins[t].at[part], outs[t].at[part], (px, py, c), landing=outs[t].at[part], first=first,
                            send_if=_is_chip(x, y, from_chip[t][0]), recv_if=_is_chip(px, py, from_chip[t][0]))
        for fx, fy in CHIP_FLIPS:
            px, py = _flip(x, fx), _flip(y, fy)
            peer = (px, py, c)
            for t, (chip, _) in enumerate(from_chip):
                if t not in split:
                    mv.send(ins[t], outs[t], peer, landing=outs[t],
                            send_if=_is_chip(x, y, chip), recv_if=_is_chip(px, py, chip))
            for t in range(n2):
                mv.send(ins[n1 + t], outs[n1 + t].at[me], peer, landing=outs[n1 + t].at[2 * px + py])

    def forward(mv, ins, outs):
        x, y, c = _position()
        for t in split:
            chip = from_chip[t][0]
            for f, (fx, fy) in enumerate(near):
                gx, gy = near[1 - f]
                part = quarters(t, c, f)
                mv.send(outs[t].at[part], outs[t].at[part], (_flip(x, gx), _flip(y, gy), c), landing=outs[t].at[part],
                        send_if=_is_chip(_flip(x, fx), _flip(y, fy), chip), recv_if=_is_chip(1 - x, 1 - y, chip))

    def to_sibling(mv, ins, outs):
        x, y, c = _position()
        for t in split:
            came = jnp.logical_not(_is_chip(x, y, from_chip[t][0]))
            mv.send(outs[t].at[quarters(t, c, 0, 2)], outs[t].at[quarters(t, c, 0, 2)], (x, y, 1 - c),
                    landing=outs[t].at[quarters(t, 1 - c, 0, 2)], send_if=came, recv_if=came)

    arrays = [a for _, a in from_chip] + list(from_all)
    shapes = [jax.ShapeDtypeStruct(a.shape, a.dtype) for _, a in from_chip]
    shapes += [jax.ShapeDtypeStruct((N_CHIPS,) + a.shape, a.dtype) for a in from_all]
    n_remote = len(CHIP_FLIPS) * (n1 - len(split) + n2) + 2 * len(near) * len(split)
    relays = ((len(near) * len(split), forward), (len(split), to_sibling)) if split else None
    return _Exchange(arrays, shapes, n_remote, n1 + n2, build, relays, in_place=split)


def _scatter_exchange(to_chip, to_all, via_neighbours=False):
    n1, n2 = len(to_chip), len(to_all)
    near = CHIP_FLIPS[:2]
    direct = near if via_neighbours else CHIP_FLIPS

    def half(t, g):
        n = to_chip[t][1].shape[0] // 2
        return pl.ds(g * n, n)

    def build(mv, ins, outs):
        x, y, c = _position()
        if via_neighbours:
            for t, (chip, _) in enumerate(to_chip):
                for g, (gx, gy) in enumerate(near):
                    ox, oy = near[1 - g]
                    mv.send(ins[t].at[half(t, g)], outs[n1 + n2 + t], (_flip(x, gx), _flip(y, gy), c),
                            landing=outs[n1 + n2 + t], first=True, send_if=_is_chip(1 - x, 1 - y, chip),
                            recv_if=_is_chip(_flip(x, ox), _flip(y, oy), chip))
        for f, (fx, fy) in enumerate(CHIP_FLIPS):
            px, py = _flip(x, fx), _flip(y, fy)
            peer = (px, py, c)
            if (fx, fy) in direct:
                for t, (chip, _) in enumerate(to_chip):
                    mv.send(ins[t], outs[t].at[f], peer, landing=outs[t].at[f],
                            send_if=_is_chip(px, py, chip), recv_if=_is_chip(x, y, chip))
            for t in range(n2):
                mv.send(ins[n1 + t].at[2 * px + py], outs[n1 + t].at[f], peer, landing=outs[n1 + t].at[f])

    def forward(mv, ins, outs):
        x, y, c = _position()
        for t, (chip, _) in enumerate(to_chip):
            for g in range(len(near)):
                ox, oy = near[1 - g]
                far_slot = outs[t].at[len(near)].at[half(t, g)]
                mv.send(outs[n1 + n2 + t], far_slot, (_flip(x, ox), _flip(y, oy), c), landing=far_slot,
                        send_if=_is_chip(_flip(x, ox), _flip(y, oy), chip), recv_if=_is_chip(x, y, chip))

    arrays = [a for _, a in to_chip] + list(to_all)
    shapes = [jax.ShapeDtypeStruct((len(CHIP_FLIPS),) + a.shape, a.dtype) for _, a in to_chip]
    shapes += [jax.ShapeDtypeStruct((len(CHIP_FLIPS),) + a.shape[1:], a.dtype) for a in to_all]
    if not via_neighbours:
        return _Exchange(arrays, shapes, len(CHIP_FLIPS) * (n1 + n2), 0, build)
    shapes += [jax.ShapeDtypeStruct((a.shape[0] // 2, a.shape[1]), a.dtype) for _, a in to_chip]
    return _Exchange(arrays, shapes, 2 * len(near) * n1 + len(CHIP_FLIPS) * n2, 0, build,
                     relays=((len(near) * n1, forward), None), n_staging=n1)


def _swap_sibling(tensors, name):
    n = len(tensors)

    def body(*refs):
        ins, outs = refs[:n], refs[n:2 * n]
        send_sems, recv_sems = refs[2 * n:]
        x, y, c = _position()
        copies = [pltpu.make_async_remote_copy(
            src_ref=ins[t], dst_ref=outs[t], send_sem=send_sems.at[t], recv_sem=recv_sems.at[t],
            device_id=(x, y, 1 - c), device_id_type=MESH) for t in range(n)]
        for cp in copies:
            cp.start()
        for cp in copies:
            cp.wait_recv()
        for cp in copies:
            cp.wait_send()

    return pl.pallas_call(
        body, name=name, in_specs=[ANY] * n, out_specs=[ANY] * n,
        out_shape=[jax.ShapeDtypeStruct(a.shape, a.dtype) for a in tensors],
        scratch_shapes=[pltpu.SemaphoreType.DMA((n,)), pltpu.SemaphoreType.DMA((n,))],
        compiler_params=pltpu.CompilerParams(has_side_effects=True),
    )(*tensors)


def _pair_halves(g):
    r, cols = g.shape
    half = r // 2

    def body(g_ref, o_ref, mine, theirs, send_sem, recv_sem, local_sem):
        x, y, c = _position()
        away = pltpu.make_async_remote_copy(
            src_ref=g_ref.at[pl.ds((1 - c) * half, half)], dst_ref=theirs, send_sem=send_sem, recv_sem=recv_sem,
            device_id=(x, y, 1 - c), device_id_type=MESH)
        kept = pltpu.make_async_copy(g_ref.at[pl.ds(c * half, half)], mine, local_sem)
        away.start()
        kept.start()
        kept.wait()
        away.wait_recv()
        o_ref[...] = (mine[...] + theirs[...]).astype(BF16)
        away.wait_send()

    return pl.pallas_call(
        body, name="pair_halves", in_specs=[ANY], out_specs=pl.BlockSpec(memory_space=pltpu.VMEM),
        out_shape=jax.ShapeDtypeStruct((half, cols), BF16),
        scratch_shapes=[pltpu.VMEM((half, cols), F32), pltpu.VMEM((half, cols), F32),
                        pltpu.SemaphoreType.DMA(()), pltpu.SemaphoreType.DMA(()), pltpu.SemaphoreType.DMA(())],
        compiler_params=pltpu.CompilerParams(has_side_effects=True, vmem_limit_bytes=VMEM_LIMIT),
    )(g)


def _allreduce_small(slab):
    stages = 3

    def body(x_ref, o_ref, buf, send_sems, recv_sems):
        x, y, c = _position()
        peers = ((1 - x, y, c), (x, 1 - y, c), (x, y, 1 - c))
        o_ref[...] = x_ref[...]
        for k, peer in enumerate(peers):
            cp = pltpu.make_async_remote_copy(src_ref=o_ref, dst_ref=buf.at[k], send_sem=send_sems.at[k],
                                              recv_sem=recv_sems.at[k], device_id=peer, device_id_type=MESH)
            cp.start()
            cp.wait()
            o_ref[...] = o_ref[...] + buf[k]

    return pl.pallas_call(
        body, name="allreduce_small",
        in_specs=[pl.BlockSpec(memory_space=pltpu.VMEM)], out_specs=pl.BlockSpec(memory_space=pltpu.VMEM),
        out_shape=jax.ShapeDtypeStruct(slab.shape, slab.dtype),
        scratch_shapes=[pltpu.VMEM((stages,) + slab.shape, slab.dtype),
                        pltpu.SemaphoreType.DMA((stages,)), pltpu.SemaphoreType.DMA((stages,))],
        compiler_params=pltpu.CompilerParams(has_side_effects=True),
    )(slab)


def _row_tile(r):
    return min(r, 256)


def _sum4(stack, recv, me):
    _, r, c = stack.shape
    tr = _row_tile(r)

    def body(me_ref, own_ref, recv_ref, o_ref):
        o_ref[...] = (((own_ref[...] + recv_ref[0].astype(F32)) + recv_ref[1].astype(F32))
                      + recv_ref[2].astype(F32))

    return pl.pallas_call(
        body, name="sum_partials",
        grid_spec=pltpu.PrefetchScalarGridSpec(
            num_scalar_prefetch=1, grid=(r // tr,),
            in_specs=[pl.BlockSpec((None, tr, c), lambda i, me_ref: (me_ref[0], i, 0)),
                      pl.BlockSpec((len(CHIP_FLIPS), tr, c), lambda i, me_ref: (0, i, 0))],
            out_specs=pl.BlockSpec((tr, c), lambda i, me_ref: (i, 0))),
        out_shape=jax.ShapeDtypeStruct((r, c), F32), compiler_params=_params("parallel"),
    )(me, stack, recv)


def _sum_block(own, recv):
    r, c = own.shape
    tr = _row_tile(r)

    def body(own_ref, recv_ref, o_ref):
        o_ref[...] = (((own_ref[...] + recv_ref[0].astype(F32)) + recv_ref[1].astype(F32))
                      + recv_ref[2].astype(F32))

    return pl.pallas_call(
        body, name="sum_block", grid=(r // tr,),
        in_specs=[pl.BlockSpec((tr, c), lambda i: (i, 0)), pl.BlockSpec((len(CHIP_FLIPS), tr, c), lambda i: (0, i, 0))],
        out_specs=pl.BlockSpec((tr, c), lambda i: (i, 0)),
        out_shape=jax.ShapeDtypeStruct((r, c), F32), compiler_params=_params("parallel"),
    )(own, recv)


def _sum_half(own, recv, core):
    r, c = own.shape
    tr = _row_tile(r // 2)
    per_half = r // 2 // tr

    def body(core_ref, own_ref, recv_ref, o_ref):
        mine = pl.program_id(0) // per_half == core_ref[0]

        @pl.when(mine)
        def _():
            o_ref[...] = (((own_ref[...] + recv_ref[0].astype(F32)) + recv_ref[1].astype(F32))
                          + recv_ref[2].astype(F32))

        @pl.when(jnp.logical_not(mine))
        def _():
            o_ref[...] = own_ref[...]

    return pl.pallas_call(
        body, name="sum_half",
        grid_spec=pltpu.PrefetchScalarGridSpec(
            num_scalar_prefetch=1, grid=(r // tr,),
            in_specs=[pl.BlockSpec((tr, c), lambda i, core: (i, 0)),
                      pl.BlockSpec((len(CHIP_FLIPS), tr, c), lambda i, core: (0, i % per_half, 0))],
            out_specs=pl.BlockSpec((tr, c), lambda i, core: (i, 0))),
        out_shape=jax.ShapeDtypeStruct((r, c), F32), compiler_params=_params("parallel"),
    )(core, own, recv)


def _adamw_math(w, g, m, v):
    m = ADAM_B1 * m + (1.0 - ADAM_B1) * g
    v = ADAM_B2 * v + (1.0 - ADAM_B2) * (g * g)
    m_hat = m / (1.0 - ADAM_B1 ** ADAM_STEP)
    v_hat = v / (1.0 - ADAM_B2 ** ADAM_STEP)
    delta = -ADAM_LR * (m_hat / (jnp.sqrt(v_hat) + ADAM_EPS) + ADAM_WD * w)
    return delta, m, v


def _adamw(w, m, v, g_parts, name):
    r, c = w.shape
    tr = _row_tile(r)
    n = len(g_parts)

    def body(*refs):
        w_ref, m_ref, v_ref = refs[:3]
        g_refs = refs[3:3 + n]
        g_out, d_out, m_out, v_out, zero_out = refs[3 + n:]
        g = g_refs[0][...]
        for ref in g_refs[1:]:
            g = g + ref[...]
        g_out[...] = g
        d_out[...], m_out[...], v_out[...] = _adamw_math(w_ref[...], g, m_ref[...], v_ref[...])
        zero_out[0] = 0

    blk = pl.BlockSpec((tr, c), lambda i: (i, 0))
    return pl.pallas_call(
        body, name=name, grid=(r // tr,), in_specs=[blk] * (3 + n),
        out_specs=[blk] * 4 + [pl.BlockSpec(memory_space=pltpu.SMEM)],
        out_shape=[jax.ShapeDtypeStruct((r, c), F32)] * 4 + [jax.ShapeDtypeStruct((1,), jnp.int32)],
        compiler_params=_params("arbitrary"),
    )(w, m, v, *g_parts)


def _adamw_small(total, w, m, v):
    sizes = [w[n].size for n in SMALL]
    flat = lambda d: [d[n].reshape(1, -1) for n in SMALL]
    k = len(SMALL)

    def body(*refs):
        total_ref, w_refs, m_refs, v_refs = refs[0], refs[1:1 + k], refs[1 + k:1 + 2 * k], refs[1 + 2 * k:1 + 3 * k]
        outs = refs[1 + 3 * k:]
        for i, size in enumerate(sizes):
            g = total_ref[i:i + 1, 0:size]
            outs[i][...] = g
            outs[k + i][...], outs[2 * k + i][...], outs[3 * k + i][...] = _adamw_math(
                w_refs[i][...], g, m_refs[i][...], v_refs[i][...])

    res = pl.pallas_call(
        body, name="adamw_small", out_shape=[jax.ShapeDtypeStruct((1, size), F32) for size in sizes] * 4,
        compiler_params=_params(),
    )(total, *flat(w), *flat(m), *flat(v))
    return [{n: res[j * k + i].reshape(w[n].shape) for i, n in enumerate(SMALL)} for j in range(4)]


SHARDED = ("w_in", "w_lora_up", "a_lora_up", "w_out_a", "w_out_b", "w_out")
ROW_SHARDED = ("w_out",)
SMALL = ("norm_g", "shift_mu", "w0", "a0", "k_k", "k_a", "r_k", "lnx_w", "lnx_b", "f_bias", "q_norm_g", "k_norm_g",
         "final_norm_g")
WEIGHTS = ("norm_g", "w_in", "shift_mu", "w_lora_up", "w0", "a_lora_up", "a0", "k_k", "k_a", "r_k", "lnx_w", "lnx_b",
           "f_bias", "q_norm_g", "k_norm_g", "w_out_a", "w_out_b", "w_out", "final_norm_g")
SLAB_ROWS = 16
SLAB_COLS = SEC


def _to_slab(named, extra=None):
    rows = [jnp.pad(named[n].reshape(1, -1), ((0, 0), (0, SLAB_COLS - named[n].size))) for n in SMALL]
    if extra is not None:
        rows.append(jnp.pad(extra.reshape(1, -1), ((0, 0), (0, SLAB_COLS - extra.size))))
    rows.append(jnp.zeros((SLAB_ROWS - len(rows), SLAB_COLS), F32))
    return jnp.concatenate(rows, axis=0)


def _by_chip(g, name):
    if name in ROW_SHARDED:
        return g.reshape(N_CHIPS, g.shape[0] // N_CHIPS, g.shape[1])
    r, c = g.shape
    return g.reshape(r, N_CHIPS, c // N_CHIPS).transpose(1, 0, 2)


def _from_chips(stack, name):
    if name in ROW_SHARDED:
        return stack.reshape(-1, stack.shape[2])
    _, r, c = stack.shape
    return stack.transpose(1, 0, 2).reshape(r, N_CHIPS * c)


def kernel(x, norm_g, w_in, shift_mu, w_lora_up, w0, a_lora_up, a0, k_k, k_a, r_k, lnx_w, lnx_b, f_bias, q_norm_g, k_norm_g, w_out_a, w_out_b, w_out, final_norm_g, loss_target, m_norm_g, m_w_in, m_shift_mu, m_w_lora_up, m_w0, m_a_lora_up, m_a0, m_k_k, m_k_a, m_r_k, m_lnx_w, m_lnx_b, m_f_bias, m_q_norm_g, m_k_norm_g, m_w_out_a, m_w_out_b, m_w_out, m_final_norm_g, v_norm_g, v_w_in, v_shift_mu, v_w_lora_up, v_w0, v_a_lora_up, v_a0, v_k_k, v_k_a, v_r_k, v_lnx_w, v_lnx_b, v_f_bias, v_q_norm_g, v_k_norm_g, v_w_out_a, v_w_out_b, v_w_out, v_final_norm_g):
    w = dict(norm_g=norm_g, w_in=w_in, shift_mu=shift_mu, w_lora_up=w_lora_up, w0=w0, a_lora_up=a_lora_up, a0=a0,
             k_k=k_k, k_a=k_a, r_k=r_k, lnx_w=lnx_w, lnx_b=lnx_b, f_bias=f_bias, q_norm_g=q_norm_g,
             k_norm_g=k_norm_g, w_out_a=w_out_a, w_out_b=w_out_b, w_out=w_out, final_norm_g=final_norm_g)
    m = dict(norm_g=m_norm_g, w_in=m_w_in, shift_mu=m_shift_mu, w_lora_up=m_w_lora_up, w0=m_w0,
             a_lora_up=m_a_lora_up, a0=m_a0, k_k=m_k_k, k_a=m_k_a, r_k=m_r_k, lnx_w=m_lnx_w, lnx_b=m_lnx_b,
             f_bias=m_f_bias, q_norm_g=m_q_norm_g, k_norm_g=m_k_norm_g, w_out_a=m_w_out_a, w_out_b=m_w_out_b,
             w_out=m_w_out, final_norm_g=m_final_norm_g)
    v = dict(norm_g=v_norm_g, w_in=v_w_in, shift_mu=v_shift_mu, w_lora_up=v_w_lora_up, w0=v_w0,
             a_lora_up=v_a_lora_up, a0=v_a0, k_k=v_k_k, k_a=v_k_a, r_k=v_r_k, lnx_w=v_lnx_w, lnx_b=v_lnx_b,
             f_bias=v_f_bias, q_norm_g=v_q_norm_g, k_norm_g=v_k_norm_g, w_out_a=v_w_out_a, w_out_b=v_w_out_b,
             w_out=v_w_out, final_norm_g=v_final_norm_g)
    shapes = {n: w[n].shape for n in WEIGHTS}

    shard = {n: w[n][0].astype(BF16) for n in SHARDED}
    late = ("w_out_a", "w_out_b", "w_out")
    loras = ("w_lora_up", "a_lora_up")
    w_in_head, w_in_tail = shard["w_in"][:, :A_TAIL], shard["w_in"][:, A_TAIL:]
    shard0, shard1_head, up_stack, aup_stack = _run_on_sequencer(_gather_exchange(
        [(0, shard["w_in"]), (1, w_in_head)], [shard[n] for n in loras], split=(0, 1)), "gather_early", 2)
    moments = (_row_major_copy(m["w_in"][0], "m_w_in_rows"), _row_major_copy(v["w_in"][0], "v_w_in_rows"))
    shard0, moments = lax.optimization_barrier((shard0, moments))
    w_a = jnp.concatenate([shard0, shard1_head], axis=1)

    def late_weights(arrived):
        shard1_tail, shard2, shard3 = arrived[:3]
        w_b = jnp.concatenate([shard1_tail, shard2[:, :B_TAIL], jnp.zeros((D_MODEL, SEC - FOX_REAL), BF16)], axis=1)
        w_g = jnp.concatenate([shard2[:, B_TAIL:], shard3], axis=1)
        return (w_b, w_g, *[_from_chips(s, n) for n, s in zip(late, arrived[3:])])

    own = {}
    cut = {"block0": lambda: own["dw_a"][:, :SHARD_COLS], "head1": lambda: own["dw_a"][:, SHARD_COLS:],
           "tail1": lambda: own["dw_b"][:, :B_HEAD],
           "block2": lambda: jnp.concatenate([own["dw_b"][:, B_HEAD:FOX_REAL], own["dw_g"][:, :G_HEAD]], axis=1),
           "block3": lambda: own["dw_g"][:, G_HEAD:]}

    def bwd_exchange(dw_b, dw_g, dwa, dwb, dwo):
        own.update(dw_b=dw_b, dw_g=dw_g)
        own.update({n: _by_chip(g, n) for n, g in zip(late, (dwa, dwb, dwo))})
        return _scatter_exchange([(1, cut["tail1"]().astype(BF16)), (2, cut["block2"]().astype(BF16)),
                                  (3, cut["block3"]().astype(BF16))], [own[n].astype(BF16) for n in late])

    def tail_exchange(dw_a, dw_up, da_up):
        own.update(dw_a=dw_a)
        own.update({n: _by_chip(g, n) for n, g in zip(loras, (dw_up, da_up))})
        pair = _pair_halves(dw_a)
        return _scatter_exchange([(0, pair[:, :SHARD_COLS]), (1, pair[:, SHARD_COLS:])],
                                 [own[n].astype(BF16) for n in loras], via_neighbours=True)

    small = {n: w[n] for n in SMALL}
    loss_vec, grad_x, grads, sent, sent_last = _device_grads(
        x[0], loss_target[0], small, w_a, _from_chips(up_stack, "w_lora_up"), _from_chips(aup_stack, "a_lora_up"),
        late_weights, _gather_exchange([(1, w_in_tail), (2, shard["w_in"]), (3, shard["w_in"])], [shard[n] for n in late]),
        bwd_exchange, tail_exchange)

    total = _allreduce_small(_to_slab(grads, extra=loss_vec))
    loss = (0.5 / D_MODEL) * jnp.sum(total[len(SMALL)])
    out_g, out_d, out_m, out_v = _adamw_small(total, w, m, v)

    xpos, ypos, cpos = _position()
    me = (2 * xpos + ypos).astype(jnp.int32).reshape(1)
    core = cpos.astype(jnp.int32).reshape(1)
    core_sum, theirs = {}, {}

    def update(n):
        m_n, v_n = moments if n == "w_in" else (m[n][0], v[n][0])
        g, d, m2, v2, zero = _adamw(w[n][0], m_n, v_n, [core_sum[n], theirs[n]], "adamw_" + n)
        out_g[n], out_d[n], out_m[n], out_v[n] = (a.reshape(shapes[n]) for a in (g, d, m2, v2))
        return zero

    sent_last, out_d["norm_g"] = lax.optimization_barrier((sent_last, out_d["norm_g"]))
    core_sum["w_in"] = lax.switch(me[0], [
        lambda: _sum_half(cut["block0"](), sent_last[0], core),
        lambda: jnp.concatenate([_sum_half(cut["head1"](), sent_last[1], core), _sum_block(cut["tail1"](), sent[0])],
                                axis=1),
        lambda: _sum_block(cut["block2"](), sent[1]),
        lambda: _sum_block(cut["block3"](), sent[2])])
    core_sum.update({n: _sum4(own[n], r, me) for n, r in zip(loras, sent_last[2:])})
    rest = ("w_in",) + loras
    theirs.update(zip(rest, _swap_sibling([core_sum[n] for n in rest], "swap_sibling")))
    after_w_in = me + [update(n) for n in rest][0]
    core_sum.update({n: _sum4(own[n], r, after_w_in) for n, r in zip(late, sent[3:])})
    theirs.update(zip(late, _swap_sibling([core_sum[n] for n in late], "swap_sibling_late")))
    for n in late:
        update(n)

    return (loss, grad_x.reshape(x.shape), *[out_g[n] for n in WEIGHTS], *[out_d[n] for n in WEIGHTS],
            *[out_m[n] for n in WEIGHTS], *[out_v[n] for n in WEIGHTS])
```

```python
import functools
import math

import jax
import jax.numpy as jnp
from jax import lax
from jax.experimental import pallas as pl
from jax.experimental.pallas import tpu as pltpu
from jax.experimental.pallas import tpu_sc as plsc

F32 = jnp.float32
BF16 = jnp.bfloat16

D_MODEL = 1024
D_HALF = 512
HEAD = 64
N_HEADS = 8
LORA = 64
RWKV_COLS = 2176
FOX_REAL = 2056
SEC = 2176
GATE_COLS = 2048
IN_COLS = 6280
N_CHIPS = 4
SHARD_COLS = IN_COLS // N_CHIPS
A_TAIL = RWKV_COLS - SHARD_COLS
B_HEAD = SHARD_COLS - A_TAIL
B_TAIL = FOX_REAL - B_HEAD
G_HEAD = SHARD_COLS - B_TAIL
RMS_EPS = 1e-6
LNX_EPS = 64e-5
ATT_SCALE = HEAD ** -0.5
NEG = -1e30

ADAM_LR = 0.001
ADAM_B1 = 0.9
ADAM_B2 = 0.999
ADAM_EPS = 1e-08
ADAM_WD = 0.01
ADAM_STEP = 10

LANES = 128
SUBLANES = 8
VMEM_LIMIT = 56 * 1024 * 1024
MESH = pl.DeviceIdType.MESH


def _params(*sem):
    return pltpu.CompilerParams(dimension_semantics=sem if sem else None, vmem_limit_bytes=VMEM_LIMIT)


def _sigmoid(x):
    return 1.0 / (1.0 + jnp.exp(-x))


def _log_sigmoid(x):
    return jnp.minimum(x, 0.0) - jnp.log(1.0 + jnp.exp(-jnp.abs(x)))


def _head_ones():
    r = lax.broadcasted_iota(jnp.int32, (LANES, LANES), 0) >> 6
    c = lax.broadcasted_iota(jnp.int32, (LANES, LANES), 1) >> 6
    return (r == c).astype(BF16)


def _split3(x):
    hi = x.astype(BF16)
    r1 = x - hi.astype(F32)
    mid = r1.astype(BF16)
    lo = (r1 - mid.astype(F32)).astype(BF16)
    return hi, mid, lo


def _exact_dot(x, ones_bf16, ones_first=False):
    out = None
    for piece in _split3(x):
        if ones_first:
            t = jnp.dot(ones_bf16, piece, preferred_element_type=F32)
        else:
            t = jnp.dot(piece, ones_bf16, preferred_element_type=F32)
        out = t if out is None else out + t
    return out


def _head_sum(x, bd):
    n = x.shape[1] // LANES
    parts = [_exact_dot(x[:, i * LANES:(i + 1) * LANES], bd) for i in range(n)]
    return parts[0] if n == 1 else jnp.concatenate(parts, axis=1)


def _dot_nt(a, b):
    return lax.dot_general(a, b, (((1,), (1,)), ((), ())), preferred_element_type=F32)


def _dot_tn(a, b):
    return lax.dot_general(a, b, (((0,), (0,)), ((), ())), preferred_element_type=F32)


def _colsum(x):
    return jnp.sum(x, axis=0, keepdims=True)


def _matmul_tn_acc(at, b, name, tk=512):
    m, k = at.shape
    n = b.shape[1]

    def body(a_ref, b_ref, o_ref):
        j = pl.program_id(0)

        @pl.when(j == 0)
        def _():
            o_ref[...] = jnp.zeros_like(o_ref)

        o_ref[...] += jnp.dot(a_ref[...], b_ref[...], preferred_element_type=F32)

    return pl.pallas_call(
        body, name=name, grid=(k // tk,),
        in_specs=[pl.BlockSpec((m, tk), lambda j: (0, j)), pl.BlockSpec((tk, n), lambda j: (j, 0))],
        out_specs=pl.BlockSpec((m, n), lambda j: (0, 0)),
        out_shape=jax.ShapeDtypeStruct((m, n), F32), compiler_params=_params("arbitrary"),
    )(at, b)


def _inproj_bwd(du_a, du_b, du_g, w_a, w_b, w_g, x, dx2, g, exchange=None, tm=512):
    s, d = x.shape
    nb = s // tm

    def body(*refs):
        ((da_ref, db_ref, dg_ref, wa_ref, wb_ref, wg_ref, x_ref, dx2_ref, g_ref), (gx_ref, gg_ref), _,
         moves) = _split_refs(refs, 9, 2, exchange)
        i = pl.program_id(0)
        if moves:
            moves.start(also=(i == 0))

        @pl.when(i == 0)
        def _():
            gg_ref[...] = jnp.zeros_like(gg_ref)

        dh = _dot_nt(da_ref[...], wa_ref[...])
        dh += _dot_nt(db_ref[...], wb_ref[...])
        dh += _dot_nt(dg_ref[...], wg_ref[...])
        xv = x_ref[...]
        r = lax.rsqrt(jnp.mean(xv * xv, axis=-1, keepdims=True) + RMS_EPS)
        xh = xv * r
        gg_ref[...] += _colsum(dh * xh)
        dxh = dh * g_ref[...]
        gx_ref[...] = dx2_ref[...] + r * (dxh - xh * jnp.mean(dxh * xh, axis=-1, keepdims=True))
        if moves:
            moves.wait(also=(i == nb - 1))

    row = lambda w: pl.BlockSpec((tm, w), lambda i: (i, 0))
    full = lambda a: pl.BlockSpec(a.shape, lambda i: (0, 0))
    once = lambda a: pl.BlockSpec(a.shape, lambda i: (0, 0), pipeline_mode=pl.Buffered(1))
    ex_in = exchange.operands if exchange else []
    ex_out = exchange.out_shapes if exchange else []
    res = pl.pallas_call(
        body, name="inproj_bwd", grid=(nb,),
        in_specs=[row(SEC), row(SEC), row(GATE_COLS), once(w_a), once(w_b), once(w_g), row(d), row(d), full(g)]
                 + [ANY] * len(ex_in),
        out_specs=[row(d), pl.BlockSpec((1, d), lambda i: (0, 0))] + [ANY] * len(ex_out),
        out_shape=[jax.ShapeDtypeStruct((s, d), F32), jax.ShapeDtypeStruct((1, d), F32)] + ex_out,
        scratch_shapes=exchange.scratch() if exchange else [],
        compiler_params=_params("arbitrary"),
    )(du_a, du_b, du_g, w_a, w_b, w_g, x, dx2, g, *ex_in)
    return res[0], res[1], list(res[2:])


def _rwkv_elementwise(ua, prev_row, first, mu, wl, w0, a0, kkw, kaw, bd):
    tm = ua.shape[0]
    rows = lax.broadcasted_iota(jnp.int32, (tm, 1), 0)
    prev = jnp.where(first, jnp.zeros_like(prev_row), prev_row)
    shifted = jnp.where(rows == 0, prev, pltpu.roll(ua, 1, 0))
    delta = shifted - ua
    us = ua + delta * mu
    r = us[:, 0:512]
    k0 = us[:, 512:1024]
    v = us[:, 1024:1536]
    lo = us[:, 1536:1664]
    gate = us[:, 1664:2176]
    lane = lax.broadcasted_iota(jnp.int32, (1, LANES), 1)
    th = jnp.tanh(lo)
    lin = jnp.where(lane < LORA, th, lo)
    ll = jnp.dot(lin.astype(BF16), wl, preferred_element_type=F32)
    sz = _sigmoid(w0 + ll[:, :512])
    e = sz * math.exp(-0.5)
    dec = jnp.exp(-e)
    a = _sigmoid(a0 + ll[:, 512:])
    kk0 = k0 * kkw
    ss = _head_sum(kk0 * kk0, bd)
    nrm = jnp.maximum(jnp.sqrt(ss), 1e-12)
    kk = kk0 / nrm
    k = k0 * (1.0 + (a - 1.0) * kaw)
    return dict(delta=delta, us=us, r=r, k0=k0, v=v, lo=lo, gate=gate, th=th, lin=lin, sz=sz, e=e, dec=dec,
                a=a, kk0=kk0, ss=ss, nrm=nrm, kk=kk, k=k)


def _rwkv_front(x, g, w_a, mu, wl, w0, a0, kkw, kaw, tm=256):
    s, d = x.shape

    def body(x_ref, g_ref, wa_ref, mu_ref, wl_ref, w0_ref, a0_ref, kkw_ref, kaw_ref,
             h_ref, ua_ref, r_ref, w_ref, k_ref, v_ref, a_ref, b_ref, gate_ref, last_row):
        i = pl.program_id(0)
        xv = x_ref[...]
        h = (xv * lax.rsqrt(jnp.mean(xv * xv, axis=-1, keepdims=True) + RMS_EPS) * g_ref[...]).astype(BF16)
        h_ref[...] = h
        ua = jnp.dot(h, wa_ref[...], preferred_element_type=F32)
        ua_ref[...] = ua

        @pl.when(i == 0)
        def _():
            last_row[...] = jnp.zeros_like(last_row)

        f = _rwkv_elementwise(ua, last_row[...], i == 0, mu_ref[...], wl_ref[...], w0_ref[...],
                              a0_ref[...], kkw_ref[...], kaw_ref[...], _head_ones())
        last_row[...] = ua[tm - 1:tm, :]
        r_ref[...] = f["r"]
        w_ref[...] = f["dec"]
        k_ref[...] = f["k"]
        v_ref[...] = f["v"]
        a_ref[...] = -f["kk"]
        b_ref[...] = f["kk"] * f["a"]
        gate_ref[...] = f["gate"]

    vec = lambda w: pl.BlockSpec((1, w), lambda i: (0, 0))
    row = lambda w: pl.BlockSpec((tm, w), lambda i: (i, 0))
    return pl.pallas_call(
        body, name="rwkv_front", grid=(s // tm,),
        in_specs=[row(d), vec(d), pl.BlockSpec(w_a.shape, lambda i: (0, 0), pipeline_mode=pl.Buffered(1)),
                  vec(SEC), pl.BlockSpec((LANES, 2 * D_HALF), lambda i: (0, 0)),
                  vec(D_HALF), vec(D_HALF), vec(D_HALF), vec(D_HALF)],
        out_specs=[row(d), row(SEC)] + [row(D_HALF)] * 7,
        out_shape=[jax.ShapeDtypeStruct((s, d), BF16), jax.ShapeDtypeStruct((s, SEC), F32)]
                  + [jax.ShapeDtypeStruct((s, D_HALF), F32)] * 7,
        scratch_shapes=[pltpu.VMEM((1, SEC), F32)],
        compiler_params=_params("arbitrary"),
    )(x, g, w_a, mu, wl, w0, a0, kkw, kaw)


SCAN_TB = 128
N_PAIRS = 4


def _pair_sum(x, left):
    s_l = jnp.sum(jnp.where(left, x, 0.0), axis=1, keepdims=True)
    s_r = jnp.sum(jnp.where(left, 0.0, x), axis=1, keepdims=True)
    return jnp.where(left, s_l, s_r)


def _pair_dot(x, row_l, row_r, left):
    s_l = jnp.sum(x * row_l, axis=1, keepdims=True)
    s_r = jnp.sum(x * row_r, axis=1, keepdims=True)
    return jnp.where(left, s_l, s_r)


def _halves(rows8):
    lane = lax.broadcasted_iota(jnp.int32, rows8.shape, 1)
    keep_left = (lane & (LANES - 1)) < HEAD
    return jnp.where(keep_left, rows8, 0.0), jnp.where(keep_left, 0.0, rows8)


def _quad_consts():
    lane = lax.broadcasted_iota(jnp.int32, (HEAD, 2 * LANES), 1)
    rowi = lax.broadcasted_iota(jnp.int32, (HEAD, 2 * LANES), 0)
    diag2 = rowi == (lane & (HEAD - 1))
    r = lax.broadcasted_iota(jnp.int32, (2 * LANES, 2 * LANES), 0) >> 6
    c = lax.broadcasted_iota(jnp.int32, (2 * LANES, 2 * LANES), 1) >> 6
    return diag2, (r == c).astype(BF16)


def _rows_to_columns(x8, diag2, bd2):
    lhs = jnp.concatenate([jnp.where(diag2, x8[i:i + 1], 0.0).astype(BF16) for i in range(SUBLANES)], axis=0)
    return jnp.dot(lhs, bd2, preferred_element_type=F32)


def _diag_rows(qtile, diag2, bd2, sub_row2):
    res = jnp.dot(qtile, bd2, preferred_element_type=F32)
    out = jnp.zeros((SUBLANES, 2 * LANES), F32)
    for i in range(SUBLANES):
        out = jnp.where(sub_row2 == i, _colsum(jnp.where(diag2, res[i * HEAD:(i + 1) * HEAD], 0.0)), out)
    return out


def _store_tile(qbuf, slot, p, i, x):
    qbuf[slot, p // 2, i * HEAD:(i + 1) * HEAD, (p % 2) * LANES:(p % 2 + 1) * LANES] = x.astype(BF16)


def _left_half():
    return lax.broadcasted_iota(jnp.int32, (HEAD, LANES), 1) < HEAD


def _split_refs(refs, n_rows, n_out, exchange):
    n_in = len(exchange.operands) if exchange else 0
    n_ex_out = len(exchange.out_shapes) if exchange else 0
    refs = list(refs)
    rows, refs = refs[:n_rows], refs[n_rows:]
    ex_in, refs = refs[:n_in], refs[n_in:]
    outs, refs = refs[:n_out], refs[n_out:]
    ex_out, refs = refs[:n_ex_out], refs[n_ex_out:]
    scratch, sems = (refs[:-3], refs[-3:]) if exchange else (refs, None)
    moves = exchange.moves(ex_in, ex_out, sems) if exchange else None
    return rows, outs, scratch, moves


def _wkv_fwd(r, w, k, a, b, v, exchange=None):
    s = r.shape[0]
    tb = SCAN_TB
    nb = s // tb

    def body(*refs):
        (r_ref, w_ref, k_ref, a_ref, b_ref, v_ref), (y_ref, st_ref), (state, vbuf, qbuf), moves = _split_refs(
            refs, 6, 2, exchange)
        g = pl.program_id(0)
        if moves:
            moves.start(also=(g == 0))

        @pl.when(g == 0)
        def _():
            state[...] = jnp.zeros_like(state)
            qbuf[...] = jnp.zeros_like(qbuf)

        left = _left_half()
        diag2, bd2 = _quad_consts()
        sub_row2 = lax.broadcasted_iota(jnp.int32, (SUBLANES, 2 * LANES), 0)
        groups = tb // SUBLANES
        quads = [slice(g2 * 2 * LANES, (g2 + 1) * 2 * LANES) for g2 in range(2)]

        def rows_of(q):
            return pl.ds(pl.multiple_of(q * SUBLANES, SUBLANES), SUBLANES)

        def v_tiles(q, slot):
            v8 = v_ref[rows_of(q), :]
            for g2 in range(2):
                vbuf[slot, g2] = _rows_to_columns(v8[:, quads[g2]], diag2, bd2)

        def chain(q, slot):
            rows8 = rows_of(q)
            a8, w8, b8, k8, r8 = (x[rows8, :] for x in (a_ref, w_ref, b_ref, k_ref, r_ref))
            pairs = [slice(p * LANES, (p + 1) * LANES) for p in range(N_PAIRS)]
            a_next = pltpu.roll(a8, SUBLANES - 1, 0)
            (a8_l, a8_r), (wa8_l, wa8_r) = _halves(a8), _halves(w8 * a_next)
            ba8 =jnp.concatenate([_pair_sum(b8[:, pr] * a_next[:, pr], left[0:SUBLANES]) for pr in pairs], axis=1)
            ka8 = jnp.concatenate([_pair_sum(k8[:, pr] * a_next[:, pr], left[0:SUBLANES]) for pr in pairs], axis=1)
            sp = [state[p] for p in range(N_PAIRS)]
            for i in range(0, SUBLANES, 2):
                r0, r1 = slice(i, i + 1), slice(i + 1, i + 2)
                sums = [(_pair_dot(sp[p], a8_l[r0, pairs[p]], a8_r[r0, pairs[p]], left),
                         _pair_dot(sp[p], wa8_l[r0, pairs[p]], wa8_r[r0, pairs[p]], left)) for p in range(N_PAIRS)]
                sa0, sa1 = [s[0] for s in sums], [s[1] for s in sums]
                for p in range(N_PAIRS):
                    pr = pairs[p]
                    inner = slice((p % 2) * LANES, (p % 2 + 1) * LANES)
                    vt0 = vbuf[slot, p // 2, i * HEAD:(i + 1) * HEAD, inner]
                    vt1 = vbuf[slot, p // 2, (i + 1) * HEAD:(i + 2) * HEAD, inner]
                    sa_next = sa1[p] + sa0[p] * ba8[r0, pr] + vt0 * ka8[r0, pr]
                    s1 = sp[p] * w8[r0, pr] + sa0[p] * b8[r0, pr] + vt0 * k8[r0, pr]
                    st_ref[q * SUBLANES + i, p] = s1
                    _store_tile(qbuf, slot, p, i, s1 * r8[r0, pr])
                    s2 = s1 * w8[r1, pr] + sa_next * b8[r1, pr] + vt1 * k8[r1, pr]
                    st_ref[q * SUBLANES + i + 1, p] = s2
                    _store_tile(qbuf, slot, p, i + 1, s2 * r8[r1, pr])
                    sp[p] = s2
            for p in range(N_PAIRS):
                state[p] = sp[p]

        def y_rows(q, slot):
            for g2 in range(2):
                y_ref[rows_of(q), quads[g2]] = _diag_rows(qbuf[slot, g2], diag2, bd2, sub_row2)

        v_tiles(0, 0)

        def two_groups(j, carry):
            q0 = 2 * j
            v_tiles(q0 + 1, 1)
            chain(q0, 0)
            y_rows(jnp.maximum(q0 - 1, 0), 1)
            v_tiles(jnp.minimum(q0 + 2, groups - 1), 0)
            chain(q0 + 1, 1)
            y_rows(q0, 0)
            return carry

        lax.fori_loop(0, groups // 2, two_groups, 0)
        y_rows(groups - 1, 1)
        if moves:
            moves.wait(also=(g == nb - 1))

    rows = pl.BlockSpec((tb, D_HALF), lambda g: (g, 0))
    ex_in = exchange.operands if exchange else []
    ex_out = exchange.out_shapes if exchange else []
    res = pl.pallas_call(
        body, name="wkv_fwd", grid=(nb,),
        in_specs=[rows] * 6 + [ANY] * len(ex_in),
        out_specs=[rows, pl.BlockSpec((tb, N_PAIRS, HEAD, LANES), lambda g: (g, 0, 0, 0))] + [ANY] * len(ex_out),
        out_shape=[jax.ShapeDtypeStruct((s, D_HALF), F32),
                   jax.ShapeDtypeStruct((s, N_PAIRS, HEAD, LANES), F32)] + ex_out,
        scratch_shapes=[pltpu.VMEM((N_PAIRS, HEAD, LANES), F32),
                        pltpu.VMEM((2, 2, SUBLANES * HEAD, 2 * LANES), F32),
                        pltpu.VMEM((2, 2, SUBLANES * HEAD, 2 * LANES), BF16)]
                       + (exchange.scratch() if exchange else []),
        compiler_params=_params("arbitrary"),
    )(r, w, k, a, b, v, *ex_in)
    return res[0], res[1], list(res[2:])


def _wkv_bwd(r, w, k, a, b, v, dy, st, exchange=None):
    s = r.shape[0]
    tb = SCAN_TB
    nb = s // tb

    def body(*refs):
        ((r_ref, w_ref, k_ref, a_ref, b_ref, v_ref, dy_ref, st_ref, before_ref),
         (dr_ref, dw_ref, dk_ref, dv_ref, da_ref, db_ref), (dstate, vbuf, qbuf, sbuf),
         moves) = _split_refs(refs, 9, 6, exchange)
        g = pl.program_id(0)
        first_block = g == nb - 1
        if moves:
            moves.start(also=(g == 0))

        @pl.when(g == 0)
        def _():
            dstate[...] = jnp.zeros_like(dstate)
            qbuf[...] = jnp.zeros_like(qbuf)

        left = _left_half()
        diag2, bd2 = _quad_consts()
        sub_row = lax.broadcasted_iota(jnp.int32, (SUBLANES, LANES), 0)
        sub_row2 = lax.broadcasted_iota(jnp.int32, (SUBLANES, 2 * LANES), 0)
        groups = tb // SUBLANES
        quads = [slice(g2 * 2 * LANES, (g2 + 1) * 2 * LANES) for g2 in range(2)]
        row_refs = (dr_ref, dw_ref, dk_ref, da_ref, db_ref)

        def rows_of(q):
            return pl.ds(pl.multiple_of(q * SUBLANES, SUBLANES), SUBLANES)

        def state_before(q, i, p):
            if i > 0:
                return st_ref[q * SUBLANES + i - 1, p]
            return jnp.where(q == 0, jnp.where(first_block, 0.0, before_ref[0, p]),
                             st_ref[jnp.maximum(q * SUBLANES - 1, 0), p])

        def column_tiles(q, slot):
            rows8 = rows_of(q)
            for kind, ref in enumerate((v_ref, dy_ref)):
                x8 = ref[rows8, :]
                for g2 in range(2):
                    vbuf[slot, kind, g2] = _rows_to_columns(x8[:, quads[g2]], diag2, bd2)
            a8 = a_ref[rows8, :]
            for i in range(SUBLANES):
                for p in range(N_PAIRS):
                    _store_tile(sbuf, 0, p, i, state_before(q, i, p) * a8[i:i + 1, p * LANES:(p + 1) * LANES])
            for g2 in range(2):
                vbuf[slot, 2, g2] = jnp.dot(sbuf[0, g2], bd2, preferred_element_type=F32)

        def chain(q, slot):
            rows8 = rows_of(q)
            a8, w8, b8, k8, r8 = (x[rows8, :] for x in (a_ref, w_ref, b_ref, k_ref, r_ref))
            b8_l, b8_r = _halves(b8)
            dsp = [dstate[p] for p in range(N_PAIRS)]
            outs = [[jnp.zeros((SUBLANES, LANES), F32) for _ in row_refs] for _ in range(N_PAIRS)]
            after = [st_ref[q * SUBLANES + SUBLANES - 1, p] for p in range(N_PAIRS)]
            for i in reversed(range(SUBLANES)):
                row = slice(i, i + 1)
                pl_ = [slice(p * LANES, (p + 1) * LANES) for p in range(N_PAIRS)]
                tile = [(p // 2, slice(i * HEAD, (i + 1) * HEAD), slice((p % 2) * LANES, (p % 2 + 1) * LANES))
                        for p in range(N_PAIRS)]
                sp = [state_before(q, i, p) for p in range(N_PAIRS)]
                dyt = [vbuf[(slot, 1) + tile[p]] for p in range(N_PAIRS)]
                ds = [dsp[p] + dyt[p] * r8[row, pl_[p]] for p in range(N_PAIRS)]
                dsa = [_pair_dot(ds[p], b8_l[row, pl_[p]], b8_r[row, pl_[p]], left) for p in range(N_PAIRS)]
                sa = [vbuf[(slot, 2) + tile[p]] for p in range(N_PAIRS)]
                for p in range(N_PAIRS):
                    ar, wr, br, kr = (x[row, pl_[p]] for x in (a8, w8, b8, k8))
                    vt = vbuf[(slot, 0) + tile[p]]
                    dsp[p] = ds[p] * wr + dsa[p] * ar
                    new = (_colsum(after[p] * dyt[p]), _colsum(ds[p] * sp[p]), _colsum(ds[p] * vt),
                           _colsum(sp[p] * dsa[p]), _colsum(ds[p] * sa[p]))
                    outs[p] = [jnp.where(sub_row == i, n, o) for n, o in zip(new, outs[p])]
                    _store_tile(qbuf, slot, p, i, ds[p] * kr)
                after = sp
            for p in range(N_PAIRS):
                dstate[p] = dsp[p]
                for ref, o in zip(row_refs, outs[p]):
                    ref[rows8, p * LANES:(p + 1) * LANES] = o

        def dv_rows(q, slot):
            for g2 in range(2):
                dv_ref[rows_of(q), quads[g2]] = _diag_rows(qbuf[slot, g2], diag2, bd2, sub_row2)

        column_tiles(groups - 1, 0)

        def two_groups(j, carry):
            q0 = groups - 1 - 2 * j
            column_tiles(q0 - 1, 1)
            chain(q0, 0)
            dv_rows(jnp.minimum(q0 + 1, groups - 1), 1)
            column_tiles(jnp.maximum(q0 - 2, 0), 0)
            chain(q0 - 1, 1)
            dv_rows(q0, 0)
            return carry

        lax.fori_loop(0, groups // 2, two_groups, 0)
        dv_rows(0, 1)
        if moves:
            moves.wait(also=(g == nb - 1))

    rows = pl.BlockSpec((tb, D_HALF), lambda g: (nb - 1 - g, 0))
    ex_in = exchange.operands if exchange else []
    ex_out = exchange.out_shapes if exchange else []
    res = pl.pallas_call(
        body, name="wkv_bwd", grid=(nb,),
        in_specs=[rows] * 7 + [pl.BlockSpec((tb, N_PAIRS, HEAD, LANES), lambda g: (nb - 1 - g, 0, 0, 0)),
                               pl.BlockSpec((1, N_PAIRS, HEAD, LANES),
                                            lambda g: (jnp.maximum((nb - 1 - g) * tb - 1, 0), 0, 0, 0))]
                 + [ANY] * len(ex_in),
        out_specs=[rows] * 6 + [ANY] * len(ex_out),
        out_shape=[jax.ShapeDtypeStruct((s, D_HALF), F32)] * 6 + ex_out,
        scratch_shapes=[pltpu.VMEM((N_PAIRS, HEAD, LANES), F32),
                        pltpu.VMEM((2, 3, 2, SUBLANES * HEAD, 2 * LANES), F32),
                        pltpu.VMEM((2, 2, SUBLANES * HEAD, 2 * LANES), BF16),
                        pltpu.VMEM((1, 2, SUBLANES * HEAD, 2 * LANES), BF16)]
                       + (exchange.scratch() if exchange else []),
        compiler_params=_params("arbitrary"),
    )(r, w, k, a, b, v, dy, st, st, *ex_in)
    return list(res[:6]), list(res[6:])


def _rwkv_post_math(y, r, k, v, gate, lw, lb, rk, bd):
    mean = _head_sum(y, bd) * (1.0 / HEAD)
    yc = y - mean
    var = _head_sum(yc * yc, bd) * (1.0 / HEAD)
    rstd = lax.rsqrt(var + LNX_EPS)
    yn = yc * rstd
    rkk = _head_sum(r * k * rk, bd)
    sg = _sigmoid(gate)
    pre = yn * lw + lb + rkk * v
    return yn, rstd, rkk, sg, pre


def _rwkv_prep_bwd(u_a, h_t, grads, mu, wl, w0, a0, kkw, kaw, tm=256):
    s = u_a.shape[0]
    nb = s // tm
    d = h_t.shape[0]

    def body(ua_ref, prev_ref, ht_ref, drs_ref, dws_ref, dks_ref, dvs_ref, das_ref, dbs_ref, drb_ref, dkb_ref, dvb_ref,
             dgt_ref, mu_ref, wl_ref, w0_ref, a0_ref, kkw_ref, kaw_ref,
             du_ref, dwa_ref, dmu_ref, dwl_ref, dw0_ref, da0_ref, dkkw_ref, dkaw_ref, carry):
        i = pl.program_id(0)

        @pl.when(i == 0)
        def _():
            carry[...] = jnp.zeros_like(carry)
            for ref in (dwa_ref, dmu_ref, dwl_ref, dw0_ref, da0_ref, dkkw_ref, dkaw_ref):
                ref[...] = jnp.zeros_like(ref)

        bd = _head_ones()
        mu_v, wl_v, kkw_v, kaw_v = mu_ref[...], wl_ref[...], kkw_ref[...], kaw_ref[...]
        f = _rwkv_elementwise(ua_ref[...], prev_ref[7:8, :], i == nb - 1, mu_v, wl_v, w0_ref[...],
                              a0_ref[...], kkw_v, kaw_v, bd)
        a, kk, k0 = f["a"], f["kk"], f["k0"]
        dk = dks_ref[...] + dkb_ref[...]
        dbs = dbs_ref[...]
        dkk = dbs * a - das_ref[...]
        da = dbs * kk + dk * k0 * kaw_v
        dk0 = dk * (1.0 + (a - 1.0) * kaw_v)
        dkaw_ref[...] += _colsum(dk * k0 * (a - 1.0))
        inv = 1.0 / f["nrm"]
        proj = _head_sum(dkk * kk, bd)
        dkk0 = jnp.where(f["ss"] > 1e-24, (dkk - kk * proj) * inv, dkk * inv)
        dk0 = dk0 + dkk0 * kkw_v
        dkkw_ref[...] += _colsum(dkk0 * k0)
        dza = da * a * (1.0 - a)
        da0_ref[...] += _colsum(dza)
        dz = -dws_ref[...] * f["dec"] * f["e"] * (1.0 - f["sz"])
        dw0_ref[...] += _colsum(dz)
        dll = jnp.concatenate([dz, dza], axis=1).astype(BF16)
        dwl_ref[...] += _dot_tn(f["lin"].astype(BF16), dll)
        dlin = _dot_nt(dll, wl_v)
        lane = lax.broadcasted_iota(jnp.int32, (1, LANES), 1)
        th = f["th"]
        dlo = jnp.where(lane < LORA, dlin * (1.0 - th * th), dlin)
        dus = jnp.concatenate([drs_ref[...] + drb_ref[...], dk0, dvs_ref[...] + dvb_ref[...], dlo, dgt_ref[...]],
                              axis=1)
        dmu_ref[...] += _colsum(dus * f["delta"])
        g1 = dus * mu_v
        rows = lax.broadcasted_iota(jnp.int32, (tm, 1), 0)
        up = jnp.where(rows == tm - 1, carry[...], pltpu.roll(g1, tm - 1, 0))
        dua = dus - g1 + up
        du = dua.astype(BF16)
        du_ref[...] = du
        dwa_ref[...] += jnp.dot(ht_ref[...], du, preferred_element_type=F32)
        carry[...] = g1[0:1, :]

    rev = lambda w: pl.BlockSpec((tm, w), lambda i: (nb - 1 - i, 0))
    vec = lambda w: pl.BlockSpec((1, w), lambda i: (0, 0))
    wl_spec = pl.BlockSpec((LANES, 2 * D_HALF), lambda i: (0, 0))
    return pl.pallas_call(
        body, name="rwkv_prep_bwd", grid=(nb,),
        in_specs=[rev(SEC), pl.BlockSpec((8, SEC), lambda i: (jnp.maximum((nb - 1 - i) * (tm // 8) - 1, 0), 0)),
                  pl.BlockSpec((d, tm), lambda i: (0, nb - 1 - i))]
                 + [rev(D_HALF)] * 10 + [vec(SEC), wl_spec] + [vec(D_HALF)] * 4,
        out_specs=[rev(SEC), pl.BlockSpec((d, SEC), lambda i: (0, 0)), vec(SEC), wl_spec] + [vec(D_HALF)] * 4,
        out_shape=[jax.ShapeDtypeStruct((s, SEC), BF16), jax.ShapeDtypeStruct((d, SEC), F32),
                   jax.ShapeDtypeStruct((1, SEC), F32),
                   jax.ShapeDtypeStruct((LANES, 2 * D_HALF), F32)] + [jax.ShapeDtypeStruct((1, D_HALF), F32)] * 4,
        scratch_shapes=[pltpu.VMEM((1, SEC), F32)],
        compiler_params=_params("arbitrary"),
    )(u_a, u_a, h_t, *grads, mu, wl, w0, a0, kkw, kaw)


def _tri(tm, lower):
    r = lax.broadcasted_iota(jnp.int32, (tm, tm), 0)
    c = lax.broadcasted_iota(jnp.int32, (tm, tm), 1)
    return ((r >= c) if lower else (r <= c)).astype(BF16)


def _head_rms(x, g, bd):
    rinv = lax.rsqrt(_head_sum(x * x, bd) * (1.0 / HEAD) + RMS_EPS)
    xh = x * rinv
    return xh, rinv, xh * g


def _fox_front(h, w_b, fb, qg, kg, tm=256):
    s, d = h.shape

    def body(h_ref, wb_ref, fb_ref, qg_ref, kg_ref, ub_ref, q_ref, k_ref, v_ref, cc_ref, cr_ref, carry):
        i = pl.program_id(0)

        @pl.when(i == 0)
        def _():
            carry[...] = jnp.zeros_like(carry)

        ub_ref[...] = jnp.dot(h_ref[...], wb_ref[...], preferred_element_type=F32)
        bd = _head_ones()
        _, _, qn = _head_rms(ub_ref[:, 0:512], qg_ref[...], bd)
        _, _, kn = _head_rms(ub_ref[:, 512:1024], kg_ref[...], bd)
        q_ref[...] = (qn * ATT_SCALE).astype(BF16)
        k_ref[...] = kn.astype(BF16)
        v_ref[...] = ub_ref[:, 1024:1536].astype(BF16)
        lane = lax.broadcasted_iota(jnp.int32, (1, LANES), 1)
        logf = jnp.where(lane < N_HEADS, _log_sigmoid(ub_ref[:, 2048:2176] + fb_ref[...]), 0.0)
        cum = _exact_dot(logf, _tri(tm, True), ones_first=True) + carry[...]
        for h in range(N_HEADS):
            cc_ref[h] = jnp.broadcast_to(cum[:, h:h + 1], (tm, LANES))
        cr_ref[...] = jnp.transpose(cum)[0:N_HEADS, :]
        carry[...] = cum[tm - 1:tm, :]

    blk = pl.BlockSpec((tm, D_HALF), lambda i: (i, 0))
    return pl.pallas_call(
        body, name="fox_front", grid=(s // tm,),
        in_specs=[pl.BlockSpec((tm, d), lambda i: (i, 0)),
                  pl.BlockSpec(w_b.shape, lambda i: (0, 0), pipeline_mode=pl.Buffered(1)),
                  pl.BlockSpec((1, LANES), lambda i: (0, 0)),
                  pl.BlockSpec((1, D_HALF), lambda i: (0, 0)), pl.BlockSpec((1, D_HALF), lambda i: (0, 0))],
        out_specs=[pl.BlockSpec((tm, SEC), lambda i: (i, 0)), blk, blk, blk,
                   pl.BlockSpec((N_HEADS, tm, LANES), lambda i: (0, i, 0)), pl.BlockSpec((N_HEADS, tm), lambda i: (0, i))],
        out_shape=[jax.ShapeDtypeStruct((s, SEC), F32)] + [jax.ShapeDtypeStruct((s, D_HALF), BF16)] * 3
                  + [jax.ShapeDtypeStruct((N_HEADS, s, LANES), F32), jax.ShapeDtypeStruct((N_HEADS, s), F32)],
        scratch_shapes=[pltpu.VMEM((1, LANES), F32)],
        compiler_params=_params("arbitrary"),
    )(h, w_b, fb, qg, kg)


ATT_T = 256


def _tiles(nblk, by_query):
    if by_query:
        pairs = [(i, j) for i in range(nblk) for j in range(i + 1)]
    else:
        pairs = [(i, j) for j in range(nblk) for i in range(j, nblk)]
    return (jnp.asarray([p[0] for p in pairs], jnp.int32), jnp.asarray([p[1] for p in pairs], jnp.int32))


def _attn_fwd(q, k, v, cc, cr):
    s = q.shape[0]
    t = ATT_T
    nblk = s // t

    def body(qi_ref, kj_ref, q_ref, k_ref, v_ref, cc_ref, cr_ref, o_ref, lse_ref, m_sc, l_sc, acc_sc):
        i = qi_ref[pl.program_id(0)]
        j = kj_ref[pl.program_id(0)]

        @pl.when(j == 0)
        def _():
            m_sc[...] = jnp.full_like(m_sc, NEG)
            l_sc[...] = jnp.zeros_like(l_sc)
            acc_sc[...] = jnp.zeros_like(acc_sc)

        def tile(on_diagonal):
            causal = _causal_tile(t) if on_diagonal else None
            left = lax.broadcasted_iota(jnp.int32, (1, LANES), 1) < HEAD
            for p in range(N_PAIRS):
                lanes = slice(p * LANES, (p + 1) * LANES)
                q2, k2, v2 = q_ref[:, lanes], k_ref[:, lanes], v_ref[:, lanes]
                acc2 = acc_sc[:, lanes]
                for e in range(2):
                    h = 2 * p + e
                    msk = left if e == 0 else jnp.logical_not(left)
                    sc = _dot_nt(jnp.where(msk, q2, jnp.zeros_like(q2)), k2)
                    sc = sc + (_wide(cc_ref[h]) - cr_ref[h:h + 1, :])
                    if on_diagonal:
                        sc = jnp.where(causal, sc, NEG)
                    m_prev = m_sc[h]
                    m_new = jnp.maximum(m_prev, jnp.max(sc, axis=1, keepdims=True))
                    alpha = jnp.exp(m_prev - m_new)
                    pm = jnp.exp(sc - _wide(m_new))
                    l_sc[h] = alpha * l_sc[h] + jnp.sum(pm, axis=1, keepdims=True)
                    m_sc[h] = m_new
                    pv = jnp.dot(pm.astype(BF16), v2, preferred_element_type=F32)
                    acc2 = jnp.where(msk, alpha * acc2 + pv, acc2)
                acc_sc[:, lanes] = acc2

        pl.when(j < i)(functools.partial(tile, False))
        pl.when(j == i)(functools.partial(tile, True))

        @pl.when(j == i)
        def _():
            left = lax.broadcasted_iota(jnp.int32, (1, LANES), 1) < HEAD
            for p in range(N_PAIRS):
                lanes = slice(p * LANES, (p + 1) * LANES)
                inv = jnp.where(left, 1.0 / l_sc[2 * p], 1.0 / l_sc[2 * p + 1])
                o_ref[:, lanes] = acc_sc[:, lanes] * inv
            for h in range(N_HEADS):
                lse_ref[h] = m_sc[h] + jnp.log(l_sc[h])

    qi, kj = _tiles(nblk, by_query=True)
    qblk = pl.BlockSpec((t, D_HALF), lambda n, qi, kj: (qi[n], 0))
    kblk = pl.BlockSpec((t, D_HALF), lambda n, qi, kj: (kj[n], 0))
    qrep = pl.BlockSpec((N_HEADS, t, LANES), lambda n, qi, kj: (0, qi[n], 0))
    return pl.pallas_call(
        body, name="fox_attn_fwd",
        grid_spec=pltpu.PrefetchScalarGridSpec(
            num_scalar_prefetch=2, grid=(qi.shape[0],),
            in_specs=[qblk, kblk, kblk, qrep, pl.BlockSpec((N_HEADS, t), lambda n, qi, kj: (0, kj[n]))],
            out_specs=[qblk, qrep],
            scratch_shapes=[pltpu.VMEM((N_HEADS, t, LANES), F32), pltpu.VMEM((N_HEADS, t, LANES), F32),
                            pltpu.VMEM((t, D_HALF), F32)]),
        out_shape=[jax.ShapeDtypeStruct((s, D_HALF), F32), jax.ShapeDtypeStruct((N_HEADS, s, LANES), F32)],
        compiler_params=_params("arbitrary"),
    )(qi, kj, q, k, v, cc, cr)


def _causal_tile(t):
    return lax.broadcasted_iota(jnp.int32, (t, t), 0) >= lax.broadcasted_iota(jnp.int32, (t, t), 1)


def _wide(x):
    return jnp.concatenate([x, x], axis=1)


def _attn_probs(q2, k2, v2, do2, msk, causal, bias, lse_rows):
    zero = jnp.zeros_like(q2)
    qh = jnp.where(msk, q2, zero)
    doh = jnp.where(msk, do2, zero)
    sc = _dot_nt(qh, k2) + bias
    if causal is not None:
        sc = jnp.where(causal, sc, NEG)
    pm = jnp.exp(sc - _wide(lse_rows))
    dp = _dot_nt(doh, v2)
    return qh, doh, pm, dp


def _attn_bwd_rowdot(q, k, v, do, lse, cc, cr):
    s = q.shape[0]
    t = ATT_T
    nblk = s // t

    def body(qi_ref, kj_ref, q_ref, k_ref, v_ref, do_ref, lse_ref, cc_ref, cr_ref, dd_ref, acc):
        i = qi_ref[pl.program_id(0)]
        j = kj_ref[pl.program_id(0)]

        @pl.when(j == 0)
        def _():
            acc[...] = jnp.zeros_like(acc)

        def tile(on_diagonal):
            causal = _causal_tile(t) if on_diagonal else None
            left = lax.broadcasted_iota(jnp.int32, (1, LANES), 1) < HEAD
            for p in range(N_PAIRS):
                lanes = slice(p * LANES, (p + 1) * LANES)
                q2, k2, v2, do2 = q_ref[:, lanes], k_ref[:, lanes], v_ref[:, lanes], do_ref[:, lanes]
                for e in range(2):
                    h = 2 * p + e
                    msk = left if e == 0 else jnp.logical_not(left)
                    bias = _wide(cc_ref[h]) - cr_ref[h:h + 1, :]
                    _, _, pm, dp = _attn_probs(q2, k2, v2, do2, msk, causal, bias, lse_ref[h])
                    acc[h] += jnp.sum(pm * dp, axis=1, keepdims=True)

        pl.when(j < i)(functools.partial(tile, False))
        pl.when(j == i)(functools.partial(tile, True))

        @pl.when(j == i)
        def _():
            dd_ref[...] = acc[...]

    qi, kj = _tiles(nblk, by_query=True)
    qblk = pl.BlockSpec((t, D_HALF), lambda n, qi, kj: (qi[n], 0))
    qcol = pl.BlockSpec((N_HEADS, t, LANES), lambda n, qi, kj: (0, qi[n], 0))
    kblk = pl.BlockSpec((t, D_HALF), lambda n, qi, kj: (kj[n], 0))
    return pl.pallas_call(
        body, name="fox_attn_rowdot",
        grid_spec=pltpu.PrefetchScalarGridSpec(
            num_scalar_prefetch=2, grid=(qi.shape[0],),
            in_specs=[qblk, kblk, kblk, qblk, qcol, qcol, pl.BlockSpec((N_HEADS, t), lambda n, qi, kj: (0, kj[n]))],
            out_specs=qcol, scratch_shapes=[pltpu.VMEM((N_HEADS, t, LANES), F32)]),
        out_shape=jax.ShapeDtypeStruct((N_HEADS, s, LANES), F32),
        compiler_params=_params("arbitrary"),
    )(qi, kj, q, k, v, do, lse, cc, cr)


def _attn_bwd(q, k, v, do, lse, dd, cc, cr):
    s = q.shape[0]
    t = ATT_T
    nblk = s // t

    def body(qi_ref, kj_ref, q_ref, k_ref, v_ref, do_ref, lse_ref, dd_ref, cc_ref, cr_ref,
             dq_ref, dk_ref, dv_ref, dcr_ref, dk_sc, dv_sc, dcr_sc):
        i = qi_ref[pl.program_id(0)]
        j = kj_ref[pl.program_id(0)]

        @pl.when(pl.program_id(0) == 0)
        def _():
            dq_ref[...] = jnp.zeros_like(dq_ref)

        @pl.when(i == j)
        def _():
            dk_sc[...] = jnp.zeros_like(dk_sc)
            dv_sc[...] = jnp.zeros_like(dv_sc)
            dcr_sc[...] = jnp.zeros_like(dcr_sc)

        def tile(on_diagonal):
            causal = _causal_tile(t) if on_diagonal else None
            left = lax.broadcasted_iota(jnp.int32, (1, LANES), 1) < HEAD
            qrows = pl.ds(pl.multiple_of(i * t, t), t)
            for p in range(N_PAIRS):
                lanes = slice(p * LANES, (p + 1) * LANES)
                q2, k2, v2, do2 = q_ref[:, lanes], k_ref[:, lanes], v_ref[:, lanes], do_ref[:, lanes]
                zero = jnp.zeros_like(q2)
                dq2 = jnp.zeros((t, LANES), F32)
                dk2 = jnp.zeros((t, LANES), F32)
                dv2 = jnp.zeros((t, LANES), F32)
                for e in range(2):
                    h = 2 * p + e
                    msk = left if e == 0 else jnp.logical_not(left)
                    bias = _wide(cc_ref[h]) - cr_ref[h:h + 1, :]
                    qh, doh, pm, dp = _attn_probs(q2, k2, v2, do2, msk, causal, bias, lse_ref[h])
                    dsc = pm * (dp - _wide(dd_ref[h]))
                    dsb = dsc.astype(BF16)
                    dv2 += _dot_tn(pm.astype(BF16), doh)
                    dk2 += _dot_tn(dsb, qh)
                    dq2 += jnp.dot(dsb, jnp.where(msk, k2, zero), preferred_element_type=F32)
                    dcr_sc[h:h + 1, :] += -_colsum(dsc)
                dq_ref[qrows, lanes] += dq2 * ATT_SCALE
                dk_sc[:, lanes] += dk2
                dv_sc[:, lanes] += dv2

        pl.when(i > j)(functools.partial(tile, False))
        pl.when(i == j)(functools.partial(tile, True))

        @pl.when(i == nblk - 1)
        def _():
            dk_ref[...] = dk_sc[...]
            dv_ref[...] = dv_sc[...]
            dcr_ref[...] = dcr_sc[...]

    qi, kj = _tiles(nblk, by_query=False)
    qblk = pl.BlockSpec((t, D_HALF), lambda n, qi, kj: (qi[n], 0))
    qcol = pl.BlockSpec((N_HEADS, t, LANES), lambda n, qi, kj: (0, qi[n], 0))
    kblk = pl.BlockSpec((t, D_HALF), lambda n, qi, kj: (kj[n], 0))
    krow = pl.BlockSpec((N_HEADS, t), lambda n, qi, kj: (0, kj[n]))
    return pl.pallas_call(
        body, name="fox_attn_bwd",
        grid_spec=pltpu.PrefetchScalarGridSpec(
            num_scalar_prefetch=2, grid=(qi.shape[0],),
            in_specs=[qblk, kblk, kblk, qblk, qcol, qcol, qcol, krow],
            out_specs=[pl.BlockSpec((s, D_HALF), lambda n, qi, kj: (0, 0)), kblk, kblk, krow],
            scratch_shapes=[pltpu.VMEM((t, D_HALF), F32), pltpu.VMEM((t, D_HALF), F32), pltpu.VMEM((N_HEADS, t), F32)]),
        out_shape=[jax.ShapeDtypeStruct((s, D_HALF), F32)] * 3 + [jax.ShapeDtypeStruct((N_HEADS, s), F32)],
        compiler_params=_params("arbitrary"),
    )(qi, kj, q, k, v, do, lse, dd, cc, cr)


def _fox_prep_bwd(u_b, h_t, dq, dk, dv, dgate, dcum, fb, qg, kg, tm=256):
    s = u_b.shape[0]
    nb = s // tm
    d = h_t.shape[0]

    def body(ub_ref, ht_ref, dq_ref, dk_ref, dv_ref, dg_ref, dc_ref, fb_ref, qg_ref, kg_ref,
             du_ref, dwb_ref, dqg_ref, dkg_ref, dfb_ref, carry):
        i = pl.program_id(0)

        @pl.when(i == 0)
        def _():
            carry[...] = jnp.zeros_like(carry)
            dwb_ref[...] = jnp.zeros_like(dwb_ref)
            dqg_ref[...] = jnp.zeros_like(dqg_ref)
            dkg_ref[...] = jnp.zeros_like(dkg_ref)
            dfb_ref[...] = jnp.zeros_like(dfb_ref)

        bd = _head_ones()
        for lo, g_ref, d_ref, dgain_ref in ((0, qg_ref, dq_ref, dqg_ref), (512, kg_ref, dk_ref, dkg_ref)):
            gain = g_ref[...]
            xh, rinv, _ = _head_rms(ub_ref[:, lo:lo + 512], gain, bd)
            dn = d_ref[...]
            dgain_ref[...] += _colsum(dn * xh)
            dxh = dn * gain
            du_ref[:, lo:lo + 512] = (rinv * (dxh - xh * (_head_sum(dxh * xh, bd) * (1.0 / HEAD)))).astype(BF16)
        du_ref[:, 1024:1536] = dv_ref[...].astype(BF16)
        du_ref[:, 1536:2048] = dg_ref[...].astype(BF16)
        lane = lax.broadcasted_iota(jnp.int32, (1, LANES), 1)
        dc = dc_ref[...]
        dlogf = _exact_dot(dc, _tri(tm, False), ones_first=True) + carry[...]
        carry[...] += _colsum(dc)
        fl = ub_ref[:, 2048:2176] + fb_ref[...]
        dfl = jnp.where(lane < N_HEADS, dlogf * (1.0 - _sigmoid(fl)), 0.0)
        du_ref[:, 2048:2176] = dfl.astype(BF16)
        dfb_ref[...] += _colsum(dfl)
        dwb_ref[...] += jnp.dot(ht_ref[...], du_ref[...], preferred_element_type=F32)

    rev = lambda w: pl.BlockSpec((tm, w), lambda i: (nb - 1 - i, 0))
    vec = lambda w: pl.BlockSpec((1, w), lambda i: (0, 0))
    return pl.pallas_call(
        body, name="fox_prep_bwd", grid=(nb,),
        in_specs=[rev(SEC), pl.BlockSpec((d, tm), lambda i: (0, nb - 1 - i))] + [rev(D_HALF)] * 4
                 + [rev(LANES), vec(LANES), vec(D_HALF), vec(D_HALF)],
        out_specs=[rev(SEC), pl.BlockSpec((d, SEC), lambda i: (0, 0)), vec(D_HALF), vec(D_HALF), vec(LANES)],
        out_shape=[jax.ShapeDtypeStruct((s, SEC), BF16), jax.ShapeDtypeStruct((d, SEC), F32),
                   jax.ShapeDtypeStruct((1, D_HALF), F32), jax.ShapeDtypeStruct((1, D_HALF), F32),
                   jax.ShapeDtypeStruct((1, LANES), F32)],
        scratch_shapes=[pltpu.VMEM((1, LANES), F32)],
        compiler_params=_params("arbitrary"),
    )(u_b, h_t, dq, dk, dv, dgate, dcum, fb, qg, kg)


def _merge(y, r, k, v, gate_a, o, u_b, h, x, tgt, w_g, wa, wb, wo, fg, lw, lb, rk, tm=256):
    s, d = x.shape

    def body(y_ref, r_ref, k_ref, v_ref, ga_ref, o_ref, gb_ref, h_ref, x_ref, t_ref, wg_ref, wa_ref, wb_ref, wo_ref,
             fg_ref, lw_ref, lb_ref, rk_ref,
             dx2_ref, dy_ref, drb_ref, dkb_ref, dvb_ref, dga_ref, do_ref, dgb_ref, dug_ref,
             dwa_ref, dwb_ref, dwo_ref, dfg_ref, loss_ref, dlw_ref, dlb_ref, drk_ref):
        i = pl.program_id(0)

        @pl.when(i == 0)
        def _():
            for ref in (dwa_ref, dwb_ref, dwo_ref, dfg_ref, loss_ref, dlw_ref, dlb_ref, drk_ref):
                ref[...] = jnp.zeros_like(ref)

        bd = _head_ones()
        wa_v, wb_v, wo_v, fg_v = wa_ref[...], wb_ref[...], wo_ref[...], fg_ref[...]
        rv, kv, vv, ga, lw_v, rk_v = r_ref[...], k_ref[...], v_ref[...], ga_ref[...], lw_ref[...], rk_ref[...]
        yn, rstd, rkk, sga, pre = _rwkv_post_math(y_ref[...], rv, kv, vv, ga, lw_v, lb_ref[...], rk_v, bd)
        silu_a = ga * sga
        gb, ov = gb_ref[...], o_ref[...]
        sgb = _sigmoid(gb)
        silu_b = gb * sgb
        ma = (pre * silu_a).astype(BF16)
        mb = (ov * silu_b).astype(BF16)
        ya = jnp.dot(ma, wa_v, preferred_element_type=F32)
        yb = jnp.dot(mb, wb_v, preferred_element_type=F32)
        ug = jnp.dot(h_ref[...], wg_ref[...], preferred_element_type=F32)
        sa = _sigmoid(ug[:, 0:d])
        sb = _sigmoid(ug[:, d:2 * d])
        merged = (sa * ya + sb * yb).astype(BF16)
        x2 = x_ref[...] + jnp.dot(merged, wo_v, preferred_element_type=F32)
        r2 = lax.rsqrt(jnp.mean(x2 * x2, axis=-1, keepdims=True) + RMS_EPS)
        x2h = x2 * r2
        err = x2h * fg_v - t_ref[...]
        loss_ref[...] += _colsum(err * err)
        dyo = err * (1.0 / d)
        dfg_ref[...] += _colsum(dyo * x2h)
        dx2h = dyo * fg_v
        dx2 = r2 * (dx2h - x2h * jnp.mean(dx2h * x2h, axis=-1, keepdims=True))
        dx2_ref[...] = dx2
        dx2b = dx2.astype(BF16)
        dmerged = _dot_nt(dx2b, wo_v)
        dwo_ref[...] += _dot_tn(merged, dx2b)
        dya = dmerged * sa
        dyb = dmerged * sb
        dug_ref[:, 0:d] = (dya * ya * (1.0 - sa)).astype(BF16)
        dug_ref[:, d:2 * d] = (dyb * yb * (1.0 - sb)).astype(BF16)
        dyab = dya.astype(BF16)
        dybb = dyb.astype(BF16)
        dwa_ref[...] += _dot_tn(ma, dyab)
        dwb_ref[...] += _dot_tn(mb, dybb)
        dmb = _dot_nt(dybb, wb_v)
        do_ref[...] = (dmb * silu_b).astype(BF16)
        dgb_ref[...] = dmb * ov * (sgb * (1.0 + gb * (1.0 - sgb)))
        dma = _dot_nt(dyab, wa_v)
        dga_ref[...] = dma * pre * (sga * (1.0 + ga * (1.0 - sga)))
        dpre = dma * silu_a
        dlw_ref[...] += _colsum(dpre * yn)
        dlb_ref[...] += _colsum(dpre)
        dyn = dpre * lw_v
        m1 = _head_sum(dyn, bd) * (1.0 / HEAD)
        m2 = _head_sum(dyn * yn, bd) * (1.0 / HEAD)
        dy_ref[...] = rstd * (dyn - m1 - yn * m2)
        dvb_ref[...] = dpre * rkk
        drkk = _head_sum(dpre * vv, bd)
        drb_ref[...] = drkk * kv * rk_v
        dkb_ref[...] = drkk * rv * rk_v
        drk_ref[...] += _colsum(drkk * rv * kv)

    row = lambda w: pl.BlockSpec((tm, w), lambda i: (i, 0))
    full = lambda a: pl.BlockSpec(a.shape, lambda i: (0, 0))
    once = lambda a: pl.BlockSpec(a.shape, lambda i: (0, 0), pipeline_mode=pl.Buffered(1))
    half = jax.ShapeDtypeStruct((s, D_HALF), F32)
    fshape = lambda a: jax.ShapeDtypeStruct(a.shape, F32)
    return pl.pallas_call(
        body, name="merge_fwd_bwd", grid=(s // tm,),
        in_specs=[row(D_HALF)] * 6 + [pl.BlockSpec((tm, D_HALF), lambda i: (i, 3)), row(d), row(d), row(d),
                                      once(w_g), once(wa), once(wb), once(wo), full(fg), full(lw), full(lb), full(rk)],
        out_specs=[row(d)] + [row(D_HALF)] * 7 + [row(GATE_COLS), full(wa), full(wb), full(wo), full(fg), full(fg),
                                                   full(lw), full(lb), full(rk)],
        out_shape=[jax.ShapeDtypeStruct((s, d), F32)] + [half] * 5 + [jax.ShapeDtypeStruct((s, D_HALF), BF16), half,
                                                                    jax.ShapeDtypeStruct((s, GATE_COLS), BF16),
                                                                    fshape(wa), fshape(wb), fshape(wo), fshape(fg),
                                                                    fshape(fg), fshape(lw), fshape(lb), fshape(rk)],
        compiler_params=_params("arbitrary"),
    )(y, r, k, v, gate_a, o, u_b, h, x, tgt, w_g, wa, wb, wo, fg, lw, lb, rk)


def _lora_weight(w_up, a_up):
    z = jnp.zeros((LORA, D_HALF), w_up.dtype)
    return jnp.concatenate([jnp.concatenate([w_up, z], axis=1), jnp.concatenate([z, a_up], axis=1)], axis=0)


def _device_grads(x, tgt, p, w_a, w_up, a_up, late_weights, fwd_exchange=None, bwd_exchange=None, tail_exchange=None):
    wl = _lora_weight(w_up, a_up)
    rk = p["r_k"].reshape(1, D_HALF)
    fb = jnp.pad(p["f_bias"], ((0, 0), (0, LANES - N_HEADS)))
    qg = jnp.tile(p["q_norm_g"], (1, N_HEADS))
    kg = jnp.tile(p["k_norm_g"], (1, N_HEADS))
    fg = p["final_norm_g"].reshape(1, D_MODEL)
    mixer = (p["shift_mu"], wl, p["w0"], p["a0"], p["k_k"], p["k_a"])

    h, u_a, r, dec, k, v, av, bv, gate_a = _rwkv_front(x, p["norm_g"], w_a, *mixer)
    y, st, arrived = _wkv_fwd(r, dec, k, av, bv, v, fwd_exchange)

    w_b, w_g, w_out_a, w_out_b, w_out = late_weights(arrived)
    u_b, q, kn, vb, cc, cr = _fox_front(h, w_b, fb, qg, kg)
    o, lse = _attn_fwd(q, kn, vb, cc, cr)

    (dx2, dy, dr_b, dk_b, dv_b, dgate_a, do, dgate_b, du_g, dwa, dwb, dwo, dfg, loss_vec, dlw, dlb, drk) = _merge(
        y, r, k, v, gate_a, o, u_b, h, x, tgt, w_g, w_out_a, w_out_b, w_out, fg, p["lnx_w"], p["lnx_b"], rk)

    dd = _attn_bwd_rowdot(q, kn, vb, do, lse, cc, cr)
    dq, dk_att, dv_att, dcr = _attn_bwd(q, kn, vb, do, lse, dd, cc, cr)
    dcum = jnp.pad(dcr.T, ((0, 0), (0, LANES - N_HEADS)))
    h_t = h.T
    du_b, dw_b, dqg, dkg, dfb = _fox_prep_bwd(u_b, h_t, dq, dk_att, dv_att, dgate_b, dcum, fb, qg, kg)
    dw_g = _matmul_tn_acc(h_t, du_g, "dw_gate")

    scan_grads, sent = _wkv_bwd(r, dec, k, av, bv, v, dy, st,
                                bwd_exchange(dw_b, dw_g, dwa, dwb, dwo) if bwd_exchange else None)
    du_a, dw_a, dmu, dwl, dw0, da0, dkkw, dkaw = _rwkv_prep_bwd(
        u_a, h_t, (*scan_grads, dr_b, dk_b, dv_b, dgate_a), *mixer)
    dw_up, da_up = dwl[:LORA, :D_HALF], dwl[LORA:, D_HALF:]
    sent_last = _run_on_sequencer(tail_exchange(dw_a, dw_up, da_up), "scatter_tail", 1) if tail_exchange else []
    grad_x, dnorm_g, _ = _inproj_bwd(du_a, du_b, du_g, w_a, w_b, w_g, x, dx2, p["norm_g"])

    grads = dict(
        norm_g=dnorm_g, w_in=(dw_a, dw_b, dw_g), shift_mu=dmu,
        w_lora_up=dw_up, w0=dw0, a_lora_up=da_up, a0=da0, k_k=dkkw, k_a=dkaw,
        r_k=drk.reshape(1, N_HEADS, HEAD), lnx_w=dlw, lnx_b=dlb, f_bias=dfb[:, :N_HEADS],
        q_norm_g=dqg.reshape(N_HEADS, HEAD).sum(axis=0, keepdims=True),
        k_norm_g=dkg.reshape(N_HEADS, HEAD).sum(axis=0, keepdims=True),
        w_out_a=dwa, w_out_b=dwb, w_out=dwo, final_norm_g=dfg.reshape(D_MODEL))
    return loss_vec, grad_x, grads, sent, sent_last


CHIP_FLIPS = ((1, 0), (0, 1), (1, 1))
ANY = pl.BlockSpec(memory_space=pl.ANY)


def _position():
    return lax.axis_index("x"), lax.axis_index("y"), lax.axis_index("c")


def _flip(v, f):
    return 1 - v if f else v


def _both(a, b):
    if a is None:
        return b
    return a if b is None else jnp.logical_and(a, b)


def _when(cond, fn):
    if cond is None:
        fn()
    else:
        pl.when(cond)(fn)


class _Moves:
    def __init__(self, send_sems, recv_sems, local_sems):
        self.send_sems, self.recv_sems, self.local_sems = send_sems, recv_sems, local_sems
        self.remote, self.local = [], []

    def send(self, src, dst, peer, landing, send_if=None, recv_if=None, first=False):
        k = len(self.remote)
        sems = dict(send_sem=self.send_sems.at[k], recv_sem=self.recv_sems.at[k], device_id=peer, device_id_type=MESH)
        out = pltpu.make_async_remote_copy(src_ref=src, dst_ref=dst, **sems)
        arrival = pltpu.make_async_remote_copy(src_ref=src, dst_ref=landing, **sems)
        self.remote.append((out, arrival, send_if, recv_if, first))

    def copy(self, src, dst, cond=None):
        cp = pltpu.make_async_copy(src, dst, self.local_sems.at[len(self.local)])
        self.local.append((cp, cond))

    def start(self, also=None):
        for cp, cond in self.local:
            _when(_both(also, cond), cp.start)
        for out, _, send_if, _, _ in self.remote:
            _when(_both(also, send_if), out.start)

    def wait_arrivals(self, also=None, first=None):
        for _, arrival, _, recv_if, is_first in self.remote:
            if first is None or first == is_first:
                _when(_both(also, recv_if), arrival.wait_recv)

    def wait_sent(self, also=None):
        for out, _, send_if, _, _ in self.remote:
            _when(_both(also, send_if), out.wait_send)
        for cp, cond in self.local:
            _when(_both(also, cond), cp.wait)

    def wait(self, also=None):
        self.wait_arrivals(also)
        self.wait_sent(also)


class _Exchange:
    def __init__(self, operands, out_shapes, n_remote, n_local, build, relays=None, in_place=(), n_staging=0):
        self.operands, self.out_shapes = list(operands), list(out_shapes)
        self.n_remote, self.n_local, self.build = n_remote, n_local, build
        self.relays, self.in_place = relays, in_place
        self.n_staging = n_staging

    def scratch(self):
        return [pltpu.SemaphoreType.DMA((self.n_remote,)), pltpu.SemaphoreType.DMA((self.n_remote,)),
                pltpu.SemaphoreType.DMA((max(self.n_local, 1),))]

    def moves(self, in_refs, out_refs, sems):
        mv = _Moves(*sems)
        self.build(mv, in_refs, out_refs)
        return mv


def _run_on_sequencer(exchange, name, collective_id):
    ins = [jax.new_ref(a, memory_space=pltpu.MemorySpace.HBM) for a in exchange.operands]
    outs = [ins[i] if i in exchange.in_place else jax.empty_ref(s, memory_space=pltpu.MemorySpace.HBM)
            for i, s in enumerate(exchange.out_shapes)]
    forward, to_sibling = exchange.relays or (None, None)
    relay_scratch = [pltpu.SemaphoreType.DMA((stage[0],)) for stage in (forward, to_sibling) if stage for _ in range(2)]

    def launch(*sems):
        x, y, c = _position()
        peers = [(_flip(x, fx), _flip(y, fy), c) for fx, fy in CHIP_FLIPS] + ([(x, y, 1 - c)] if to_sibling else [])
        barrier = pltpu.get_barrier_semaphore()
        for peer in peers:
            pl.semaphore_signal(barrier, inc=1, device_id=peer, device_id_type=MESH)
        pl.semaphore_wait(barrier, len(peers))
        moves = exchange.moves(ins, outs, sems[:3])
        moves.start()
        later = []
        if forward:
            onward = _Moves(sems[3], sems[4], None)
            forward[1](onward, ins, outs)
            moves.wait_arrivals(first=True)
            onward.start()
            moves.wait_arrivals(first=False)
            onward.wait_arrivals()
            later.append(onward)
        else:
            moves.wait_arrivals()
        if to_sibling:
            passed = _Moves(*sems[-2:], None)
            to_sibling[1](passed, ins, outs)
            passed.start()
            passed.wait_arrivals()
            later.append(passed)
        for mv in later + [moves]:
            mv.wait_sent()

    pl.kernel(launch, mesh=plsc.ScalarSubcoreMesh(axis_name="sequencer", num_cores=1), name=name,
              scratch_types=tuple(exchange.scratch() + relay_scratch),
              compiler_params=pltpu.CompilerParams(collective_id=collective_id))()
    return [o[...] for o in outs[:len(outs) - exchange.n_staging]]


def _row_major_copy(a, name):
    r, c = a.shape
    tr = _row_tile(r)

    def body(a_ref, o_ref):
        o_ref[...] = a_ref[...]

    blk = pl.BlockSpec((tr, c), lambda i: (i, 0))
    return pl.pallas_call(body, name=name, grid=(r // tr,), in_specs=[blk], out_specs=blk,
                          out_shape=jax.ShapeDtypeStruct(a.shape, a.dtype), compiler_params=_params("parallel"))(a)


def _is_chip(x, y, chip):
    return jnp.logical_and(x == chip // 2, y == chip % 2)


def _gather_exchange(from_chip, from_all, split=()):
    n1, n2 = len(from_chip), len(from_all)
    near = CHIP_FLIPS[:2]

    def quarters(t, c, first, count=1):
        n = from_chip[t][1].shape[0] // 4
        return pl.ds((2 * c + first) * n, count * n)

    def build(mv, ins, outs):
        x, y, c = _position()
        me = 2 * x + y
        for t, (chip, _) in enumerate(from_chip):
            if t not in split:
                mv.copy(ins[t], outs[t], cond=_is_chip(x, y, chip))
        for t in range(n2):
            mv.copy(ins[n1 + t], outs[n1 + t].at[me])
        for t in split:
            for first in (True, False):
                for f, (fx, fy) in enumerate(near):
                    px, py = _flip(x, fx), _flip(y, fy)
                    part = quarters(t, c, f if first else 1 - f)
                    mv.send(ins[t].at[part], outs[t].at[part], (px, py, c), landing=outs[t].at[part], first=first,
                            send_if=_is_chip(x, y, from_chip[t][0]), recv_if=_is_chip(px, py, from_chip[t][0]))
        for fx, fy in CHIP_FLIPS:
            px, py = _flip(x, fx), _flip(y, fy)
            peer = (px, py, c)
            for t, (chip, _) in enumerate(from_chip):
                if t not in split:
                    mv.send(ins[t], outs[t], peer, landing=outs[t],
                            send_if=_is_chip(x, y, chip), recv_if=_is_chip(px, py, chip))
            for t in range(n2):
                mv.send(ins[n1 + t], outs[n1 + t].at[me], peer, landing=outs[n1 + t].at[2 * px + py])

    def forward(mv, ins, outs):
        x, y, c = _position()
        for t in split:
            chip = from_chip[t][0]
            for f, (fx, fy) in enumerate(near):
                gx, gy = near[1 - f]
                part = quarters(t, c, f)
                mv.send(outs[t].at[part], outs[t].at[part], (_flip(x, gx), _flip(y, gy), c), landing=outs[t].at[part],
                        send_if=_is_chip(_flip(x, fx), _flip(y, fy), chip), recv_if=_is_chip(1 - x, 1 - y, chip))

    def to_sibling(mv, ins, outs):
        x, y, c = _position()
        for t in split:
            came = jnp.logical_not(_is_chip(x, y, from_chip[t][0]))
            mv.send(outs[t].at[quarters(t, c, 0, 2)], outs[t].at[quarters(t, c, 0, 2)], (x, y, 1 - c),
                    landing=outs[t].at[quarters(t, 1 - c, 0, 2)], send_if=came, recv_if=came)

    arrays = [a for _, a in from_chip] + list(from_all)
    shapes = [jax.ShapeDtypeStruct(a.shape, a.dtype) for _, a in from_chip]
    shapes += [jax.ShapeDtypeStruct((N_CHIPS,) + a.shape, a.dtype) for a in from_all]
    n_remote = len(CHIP_FLIPS) * (n1 - len(split) + n2) + 2 * len(near) * len(split)
    relays = ((len(near) * len(split), forward), (len(split), to_sibling)) if split else None
    return _Exchange(arrays, shapes, n_remote, n1 + n2, build, relays, in_place=split)


def _scatter_exchange(to_chip, to_all, via_neighbours=False):
    n1, n2 = len(to_chip), len(to_all)
    near = CHIP_FLIPS[:2]
    direct = near if via_neighbours else CHIP_FLIPS

    def half(t, g):
        n = to_chip[t][1].shape[0] // 2
        return pl.ds(g * n, n)

    def build(mv, ins, outs):
        x, y, c = _position()
        if via_neighbours:
            for t, (chip, _) in enumerate(to_chip):
                for g, (gx, gy) in enumerate(near):
                    ox, oy = near[1 - g]
                    mv.send(ins[t].at[half(t, g)], outs[n1 + n2 + t], (_flip(x, gx), _flip(y, gy), c),
                            landing=outs[n1 + n2 + t], first=True, send_if=_is_chip(1 - x, 1 - y, chip),
                            recv_if=_is_chip(_flip(x, ox), _flip(y, oy), chip))
        for f, (fx, fy) in enumerate(CHIP_FLIPS):
            px, py = _flip(x, fx), _flip(y, fy)
            peer = (px, py, c)
            if (fx, fy) in direct:
                for t, (chip, _) in enumerate(to_chip):
                    mv.send(ins[t], outs[t].at[f], peer, landing=outs[t].at[f],
                            send_if=_is_chip(px, py, chip), recv_if=_is_chip(x, y, chip))
            for t in range(n2):
                mv.send(ins[n1 + t].at[2 * px + py], outs[n1 + t].at[f], peer, landing=outs[n1 + t].at[f])

    def forward(mv, ins, outs):
        x, y, c = _position()
        for t, (chip, _) in enumerate(to_chip):
            for g in range(len(near)):
                ox, oy = near[1 - g]
                far_slot = outs[t].at[len(near)].at[half(t, g)]
                mv.send(outs[n1 + n2 + t], far_slot, (_flip(x, ox), _flip(y, oy), c), landing=far_slot,
                        send_if=_is_chip(_flip(x, ox), _flip(y, oy), chip), recv_if=_is_chip(x, y, chip))

    arrays = [a for _, a in to_chip] + list(to_all)
    shapes = [jax.ShapeDtypeStruct((len(CHIP_FLIPS),) + a.shape, a.dtype) for _, a in to_chip]
    shapes += [jax.ShapeDtypeStruct((len(CHIP_FLIPS),) + a.shape[1:], a.dtype) for a in to_all]
    if not via_neighbours:
        return _Exchange(arrays, shapes, len(CHIP_FLIPS) * (n1 + n2), 0, build)
    shapes += [jax.ShapeDtypeStruct((a.shape[0] // 2, a.shape[1]), a.dtype) for _, a in to_chip]
    return _Exchange(arrays, shapes, 2 * len(near) * n1 + len(CHIP_FLIPS) * n2, 0, build,
                     relays=((len(near) * n1, forward), None), n_staging=n1)


def _swap_sibling(tensors, name):
    n = len(tensors)

    def body(*refs):
        ins, outs = refs[:n], refs[n:2 * n]
        send_sems, recv_sems = refs[2 * n:]
        x, y, c = _position()
        copies = [pltpu.make_async_remote_copy(
            src_ref=ins[t], dst_ref=outs[t], send_sem=send_sems.at[t], recv_sem=recv_sems.at[t],
            device_id=(x, y, 1 - c), device_id_type=MESH) for t in range(n)]
        for cp in copies:
            cp.start()
        for cp in copies:
            cp.wait_recv()
        for cp in copies:
            cp.wait_send()

    return pl.pallas_call(
        body, name=name, in_specs=[ANY] * n, out_specs=[ANY] * n,
        out_shape=[jax.ShapeDtypeStruct(a.shape, a.dtype) for a in tensors],
        scratch_shapes=[pltpu.SemaphoreType.DMA((n,)), pltpu.SemaphoreType.DMA((n,))],
        compiler_params=pltpu.CompilerParams(has_side_effects=True),
    )(*tensors)


def _pair_halves(g):
    r, cols = g.shape
    half = r // 2

    def body(g_ref, o_ref, mine, theirs, send_sem, recv_sem, local_sem):
        x, y, c = _position()
        away = pltpu.make_async_remote_copy(
            src_ref=g_ref.at[pl.ds((1 - c) * half, half)], dst_ref=theirs, send_sem=send_sem, recv_sem=recv_sem,
            device_id=(x, y, 1 - c), device_id_type=MESH)
        kept = pltpu.make_async_copy(g_ref.at[pl.ds(c * half, half)], mine, local_sem)
        away.start()
        kept.start()
        kept.wait()
        away.wait_recv()
        o_ref[...] = (mine[...] + theirs[...]).astype(BF16)
        away.wait_send()

    return pl.pallas_call(
        body, name="pair_halves", in_specs=[ANY], out_specs=pl.BlockSpec(memory_space=pltpu.VMEM),
        out_shape=jax.ShapeDtypeStruct((half, cols), BF16),
        scratch_shapes=[pltpu.VMEM((half, cols), F32), pltpu.VMEM((half, cols), F32),
                        pltpu.SemaphoreType.DMA(()), pltpu.SemaphoreType.DMA(()), pltpu.SemaphoreType.DMA(())],
        compiler_params=pltpu.CompilerParams(has_side_effects=True, vmem_limit_bytes=VMEM_LIMIT),
    )(g)


def _allreduce_small(slab):
    stages = 3

    def body(x_ref, o_ref, buf, send_sems, recv_sems):
        x, y, c = _position()
        peers = ((1 - x, y, c), (x, 1 - y, c), (x, y, 1 - c))
        o_ref[...] = x_ref[...]
        for k, peer in enumerate(peers):
            cp = pltpu.make_async_remote_copy(src_ref=o_ref, dst_ref=buf.at[k], send_sem=send_sems.at[k],
                                              recv_sem=recv_sems.at[k], device_id=peer, device_id_type=MESH)
            cp.start()
            cp.wait()
            o_ref[...] = o_ref[...] + buf[k]

    return pl.pallas_call(
        body, name="allreduce_small",
        in_specs=[pl.BlockSpec(memory_space=pltpu.VMEM)], out_specs=pl.BlockSpec(memory_space=pltpu.VMEM),
        out_shape=jax.ShapeDtypeStruct(slab.shape, slab.dtype),
        scratch_shapes=[pltpu.VMEM((stages,) + slab.shape, slab.dtype),
                        pltpu.SemaphoreType.DMA((stages,)), pltpu.SemaphoreType.DMA((stages,))],
        compiler_params=pltpu.CompilerParams(has_side_effects=True),
    )(slab)


def _row_tile(r):
    return min(r, 256)


def _sum4(stack, recv, me):
    _, r, c = stack.shape
    tr = _row_tile(r)

    def body(me_ref, own_ref, recv_ref, o_ref):
        o_ref[...] = (((own_ref[...] + recv_ref[0].astype(F32)) + recv_ref[1].astype(F32))
                      + recv_ref[2].astype(F32))

    return pl.pallas_call(
        body, name="sum_partials",
        grid_spec=pltpu.PrefetchScalarGridSpec(
            num_scalar_prefetch=1, grid=(r // tr,),
            in_specs=[pl.BlockSpec((None, tr, c), lambda i, me_ref: (me_ref[0], i, 0)),
                      pl.BlockSpec((len(CHIP_FLIPS), tr, c), lambda i, me_ref: (0, i, 0))],
            out_specs=pl.BlockSpec((tr, c), lambda i, me_ref: (i, 0))),
        out_shape=jax.ShapeDtypeStruct((r, c), F32), compiler_params=_params("parallel"),
    )(me, stack, recv)


def _sum_block(own, recv):
    r, c = own.shape
    tr = _row_tile(r)

    def body(own_ref, recv_ref, o_ref):
        o_ref[...] = (((own_ref[...] + recv_ref[0].astype(F32)) + recv_ref[1].astype(F32))
                      + recv_ref[2].astype(F32))

    return pl.pallas_call(
        body, name="sum_block", grid=(r // tr,),
        in_specs=[pl.BlockSpec((tr, c), lambda i: (i, 0)), pl.BlockSpec((len(CHIP_FLIPS), tr, c), lambda i: (0, i, 0))],
        out_specs=pl.BlockSpec((tr, c), lambda i: (i, 0)),
        out_shape=jax.ShapeDtypeStruct((r, c), F32), compiler_params=_params("parallel"),
    )(own, recv)


def _sum_half(own, recv, core):
    r, c = own.shape
    tr = _row_tile(r // 2)
    per_half = r // 2 // tr

    def body(core_ref, own_ref, recv_ref, o_ref):
        mine = pl.program_id(0) // per_half == core_ref[0]

        @pl.when(mine)
        def _():
            o_ref[...] = (((own_ref[...] + recv_ref[0].astype(F32)) + recv_ref[1].astype(F32))
                          + recv_ref[2].astype(F32))

        @pl.when(jnp.logical_not(mine))
        def _():
            o_ref[...] = own_ref[...]

    return pl.pallas_call(
        body, name="sum_half",
        grid_spec=pltpu.PrefetchScalarGridSpec(
            num_scalar_prefetch=1, grid=(r // tr,),
            in_specs=[pl.BlockSpec((tr, c), lambda i, core: (i, 0)),
                      pl.BlockSpec((len(CHIP_FLIPS), tr, c), lambda i, core: (0, i % per_half, 0))],
            out_specs=pl.BlockSpec((tr, c), lambda i, core: (i, 0))),
        out_shape=jax.ShapeDtypeStruct((r, c), F32), compiler_params=_params("parallel"),
    )(core, own, recv)


def _adamw_math(w, g, m, v):
    m = ADAM_B1 * m + (1.0 - ADAM_B1) * g
    v = ADAM_B2 * v + (1.0 - ADAM_B2) * (g * g)
    m_hat = m / (1.0 - ADAM_B1 ** ADAM_STEP)
    v_hat = v / (1.0 - ADAM_B2 ** ADAM_STEP)
    delta = -ADAM_LR * (m_hat / (jnp.sqrt(v_hat) + ADAM_EPS) + ADAM_WD * w)
    return delta, m, v


def _adamw(w, m, v, g_parts, name):
    r, c = w.shape
    tr = _row_tile(r)
    n = len(g_parts)

    def body(*refs):
        w_ref, m_ref, v_ref = refs[:3]
        g_refs = refs[3:3 + n]
        g_out, d_out, m_out, v_out, zero_out = refs[3 + n:]
        g = g_refs[0][...]
        for ref in g_refs[1:]:
            g = g + ref[...]
        g_out[...] = g
        d_out[...], m_out[...], v_out[...] = _adamw_math(w_ref[...], g, m_ref[...], v_ref[...])
        zero_out[0] = 0

    blk = pl.BlockSpec((tr, c), lambda i: (i, 0))
    return pl.pallas_call(
        body, name=name, grid=(r // tr,), in_specs=[blk] * (3 + n),
        out_specs=[blk] * 4 + [pl.BlockSpec(memory_space=pltpu.SMEM)],
        out_shape=[jax.ShapeDtypeStruct((r, c), F32)] * 4 + [jax.ShapeDtypeStruct((1,), jnp.int32)],
        compiler_params=_params("arbitrary"),
    )(w, m, v, *g_parts)


def _adamw_small(total, w, m, v):
    sizes = [w[n].size for n in SMALL]
    flat = lambda d: [d[n].reshape(1, -1) for n in SMALL]
    k = len(SMALL)

    def body(*refs):
        total_ref, w_refs, m_refs, v_refs = refs[0], refs[1:1 + k], refs[1 + k:1 + 2 * k], refs[1 + 2 * k:1 + 3 * k]
        outs = refs[1 + 3 * k:]
        for i, size in enumerate(sizes):
            g = total_ref[i:i + 1, 0:size]
            outs[i][...] = g
            outs[k + i][...], outs[2 * k + i][...], outs[3 * k + i][...] = _adamw_math(
                w_refs[i][...], g, m_refs[i][...], v_refs[i][...])

    res = pl.pallas_call(
        body, name="adamw_small", out_shape=[jax.ShapeDtypeStruct((1, size), F32) for size in sizes] * 4,
        compiler_params=_params(),
    )(total, *flat(w), *flat(m), *flat(v))
    return [{n: res[j * k + i].reshape(w[n].shape) for i, n in enumerate(SMALL)} for j in range(4)]


SHARDED = ("w_in", "w_lora_up", "a_lora_up", "w_out_a", "w_out_b", "w_out")
ROW_SHARDED = ("w_out",)
SMALL = ("norm_g", "shift_mu", "w0", "a0", "k_k", "k_a", "r_k", "lnx_w", "lnx_b", "f_bias", "q_norm_g", "k_norm_g",
         "final_norm_g")
WEIGHTS = ("norm_g", "w_in", "shift_mu", "w_lora_up", "w0", "a_lora_up", "a0", "k_k", "k_a", "r_k", "lnx_w", "lnx_b",
           "f_bias", "q_norm_g", "k_norm_g", "w_out_a", "w_out_b", "w_out", "final_norm_g")
SLAB_ROWS = 16
SLAB_COLS = SEC


def _to_slab(named, extra=None):
    rows = [jnp.pad(named[n].reshape(1, -1), ((0, 0), (0, SLAB_COLS - named[n].size))) for n in SMALL]
    if extra is not None:
        rows.append(jnp.pad(extra.reshape(1, -1), ((0, 0), (0, SLAB_COLS - extra.size))))
    rows.append(jnp.zeros((SLAB_ROWS - len(rows), SLAB_COLS), F32))
    return jnp.concatenate(rows, axis=0)


def _by_chip(g, name):
    if name in ROW_SHARDED:
        return g.reshape(N_CHIPS, g.shape[0] // N_CHIPS, g.shape[1])
    r, c = g.shape
    return g.reshape(r, N_CHIPS, c // N_CHIPS).transpose(1, 0, 2)


def _from_chips(stack, name):
    if name in ROW_SHARDED:
        return stack.reshape(-1, stack.shape[2])
    _, r, c = stack.shape
    return stack.transpose(1, 0, 2).reshape(r, N_CHIPS * c)


def kernel(x, norm_g, w_in, shift_mu, w_lora_up, w0, a_lora_up, a0, k_k, k_a, r_k, lnx_w, lnx_b, f_bias, q_norm_g, k_norm_g, w_out_a, w_out_b, w_out, final_norm_g, loss_target, m_norm_g, m_w_in, m_shift_mu, m_w_lora_up, m_w0, m_a_lora_up, m_a0, m_k_k, m_k_a, m_r_k, m_lnx_w, m_lnx_b, m_f_bias, m_q_norm_g, m_k_norm_g, m_w_out_a, m_w_out_b, m_w_out, m_final_norm_g, v_norm_g, v_w_in, v_shift_mu, v_w_lora_up, v_w0, v_a_lora_up, v_a0, v_k_k, v_k_a, v_r_k, v_lnx_w, v_lnx_b, v_f_bias, v_q_norm_g, v_k_norm_g, v_w_out_a, v_w_out_b, v_w_out, v_final_norm_g):
    w = dict(norm_g=norm_g, w_in=w_in, shift_mu=shift_mu, w_lora_up=w_lora_up, w0=w0, a_lora_up=a_lora_up, a0=a0,
             k_k=k_k, k_a=k_a, r_k=r_k, lnx_w=lnx_w, lnx_b=lnx_b, f_bias=f_bias, q_norm_g=q_norm_g,
             k_norm_g=k_norm_g, w_out_a=w_out_a, w_out_b=w_out_b, w_out=w_out, final_norm_g=final_norm_g)
    m = dict(norm_g=m_norm_g, w_in=m_w_in, shift_mu=m_shift_mu, w_lora_up=m_w_lora_up, w0=m_w0,
             a_lora_up=m_a_lora_up, a0=m_a0, k_k=m_k_k, k_a=m_k_a, r_k=m_r_k, lnx_w=m_lnx_w, lnx_b=m_lnx_b,
             f_bias=m_f_bias, q_norm_g=m_q_norm_g, k_norm_g=m_k_norm_g, w_out_a=m_w_out_a, w_out_b=m_w_out_b,
             w_out=m_w_out, final_norm_g=m_final_norm_g)
    v = dict(norm_g=v_norm_g, w_in=v_w_in, shift_mu=v_shift_mu, w_lora_up=v_w_lora_up, w0=v_w0,
             a_lora_up=v_a_lora_up, a0=v_a0, k_k=v_k_k, k_a=v_k_a, r_k=v_r_k, lnx_w=v_lnx_w, lnx_b=v_lnx_b,
             f_bias=v_f_bias, q_norm_g=v_q_norm_g, k_norm_g=v_k_norm_g, w_out_a=v_w_out_a, w_out_b=v_w_out_b,
             w_out=v_w_out, final_norm_g=v_final_norm_g)
    shapes = {n: w[n].shape for n in WEIGHTS}

    shard = {n: w[n][0].astype(BF16) for n in SHARDED}
    late = ("w_out_a", "w_out_b", "w_out")
    loras = ("w_lora_up", "a_lora_up")
    w_in_head, w_in_tail = shard["w_in"][:, :A_TAIL], shard["w_in"][:, A_TAIL:]
    shard0, shard1_head, up_stack, aup_stack = _run_on_sequencer(_gather_exchange(
        [(0, shard["w_in"]), (1, w_in_head)], [shard[n] for n in loras], split=(0, 1)), "gather_early", 2)
    moments = (_row_major_copy(m["w_in"][0], "m_w_in_rows"), _row_major_copy(v["w_in"][0], "v_w_in_rows"))
    shard0, moments = lax.optimization_barrier((shard0, moments))
    w_a = jnp.concatenate([shard0, shard1_head], axis=1)

    def late_weights(arrived):
        shard1_tail, shard2, shard3 = arrived[:3]
        w_b = jnp.concatenate([shard1_tail, shard2[:, :B_TAIL], jnp.zeros((D_MODEL, SEC - FOX_REAL), BF16)], axis=1)
        w_g = jnp.concatenate([shard2[:, B_TAIL:], shard3], axis=1)
        return (w_b, w_g, *[_from_chips(s, n) for n, s in zip(late, arrived[3:])])

    own = {}
    cut = {"block0": lambda: own["dw_a"][:, :SHARD_COLS], "head1": lambda: own["dw_a"][:, SHARD_COLS:],
           "tail1": lambda: own["dw_b"][:, :B_HEAD],
           "block2": lambda: jnp.concatenate([own["dw_b"][:, B_HEAD:FOX_REAL], own["dw_g"][:, :G_HEAD]], axis=1),
           "block3": lambda: own["dw_g"][:, G_HEAD:]}

    def bwd_exchange(dw_b, dw_g, dwa, dwb, dwo):
        own.update(dw_b=dw_b, dw_g=dw_g)
        own.update({n: _by_chip(g, n) for n, g in zip(late, (dwa, dwb, dwo))})
        return _scatter_exchange([(1, cut["tail1"]().astype(BF16)), (2, cut["block2"]().astype(BF16)),
                                  (3, cut["block3"]().astype(BF16))], [own[n].astype(BF16) for n in late])

    def tail_exchange(dw_a, dw_up, da_up):
        own.update(dw_a=dw_a)
        own.update({n: _by_chip(g, n) for n, g in zip(loras, (dw_up, da_up))})
        pair = _pair_halves(dw_a)
        return _scatter_exchange([(0, pair[:, :SHARD_COLS]), (1, pair[:, SHARD_COLS:])],
                                 [own[n].astype(BF16) for n in loras], via_neighbours=True)

    small = {n: w[n] for n in SMALL}
    loss_vec, grad_x, grads, sent, sent_last = _device_grads(
        x[0], loss_target[0], small, w_a, _from_chips(up_stack, "w_lora_up"), _from_chips(aup_stack, "a_lora_up"),
        late_weights, _gather_exchange([(1, w_in_tail), (2, shard["w_in"]), (3, shard["w_in"])], [shard[n] for n in late]),
        bwd_exchange, tail_exchange)

    total = _allreduce_small(_to_slab(grads, extra=loss_vec))
    loss = (0.5 / D_MODEL) * jnp.sum(total[len(SMALL)])
    out_g, out_d, out_m, out_v = _adamw_small(total, w, m, v)

    xpos, ypos, cpos = _position()
    me = (2 * xpos + ypos).astype(jnp.int32).reshape(1)
    core = cpos.astype(jnp.int32).reshape(1)
    core_sum, theirs = {}, {}

    def update(n):
        m_n, v_n = moments if n == "w_in" else (m[n][0], v[n][0])
        g, d, m2, v2, zero = _adamw(w[n][0], m_n, v_n, [core_sum[n], theirs[n]], "adamw_" + n)
        out_g[n], out_d[n], out_m[n], out_v[n] = (a.reshape(shapes[n]) for a in (g, d, m2, v2))
        return zero

    sent_last, out_d["norm_g"] = lax.optimization_barrier((sent_last, out_d["norm_g"]))
    core_sum["w_in"] = lax.switch(me[0], [
        lambda: _sum_half(cut["block0"](), sent_last[0], core),
        lambda: jnp.concatenate([_sum_half(cut["head1"](), sent_last[1], core), _sum_block(cut["tail1"](), sent[0])],
                                axis=1),
        lambda: _sum_block(cut["block2"](), sent[1]),
        lambda: _sum_block(cut["block3"](), sent[2])])
    core_sum.update({n: _sum4(own[n], r, me) for n, r in zip(loras, sent_last[2:])})
    rest = ("w_in",) + loras
    theirs.update(zip(rest, _swap_sibling([core_sum[n] for n in rest], "swap_sibling")))
    after_w_in = me + [update(n) for n in rest][0]
    core_sum.update({n: _sum4(own[n], r, after_w_in) for n, r in zip(late, sent[3:])})
    theirs.update(zip(late, _swap_sibling([core_sum[n] for n in late], "swap_sibling_late")))
    for n in late:
        update(n)

    return (loss, grad_x.reshape(x.shape), *[out_g[n] for n in WEIGHTS], *[out_d[n] for n in WEIGHTS],
            *[out_m[n] for n in WEIGHTS], *[out_v[n] for n in WEIGHTS])
```

```python
import functools
import math

import jax
import jax.numpy as jnp
from jax import lax
from jax.experimental import pallas as pl
from jax.experimental.pallas import tpu as pltpu
from jax.experimental.pallas import tpu_sc as plsc

F32 = jnp.float32
BF16 = jnp.bfloat16

D_MODEL = 1024
D_HALF = 512
HEAD = 64
N_HEADS = 8
LORA = 64
RWKV_COLS = 2176
FOX_REAL = 2056
SEC = 2176
GATE_COLS = 2048
IN_COLS = 6280
N_CHIPS = 4
SHARD_COLS = IN_COLS // N_CHIPS
A_TAIL = RWKV_COLS - SHARD_COLS
B_HEAD = SHARD_COLS - A_TAIL
B_TAIL = FOX_REAL - B_HEAD
G_HEAD = SHARD_COLS - B_TAIL
RMS_EPS = 1e-6
LNX_EPS = 64e-5
ATT_SCALE = HEAD ** -0.5
NEG = -1e30

ADAM_LR = 0.001
ADAM_B1 = 0.9
ADAM_B2 = 0.999
ADAM_EPS = 1e-08
ADAM_WD = 0.01
ADAM_STEP = 10

LANES = 128
SUBLANES = 8
VMEM_LIMIT = 56 * 1024 * 1024
MESH = pl.DeviceIdType.MESH


def _params(*sem):
    return pltpu.CompilerParams(dimension_semantics=sem if sem else None, vmem_limit_bytes=VMEM_LIMIT)


def _sigmoid(x):
    return 1.0 / (1.0 + jnp.exp(-x))


def _log_sigmoid(x):
    return jnp.minimum(x, 0.0) - jnp.log(1.0 + jnp.exp(-jnp.abs(x)))


def _head_ones():
    r = lax.broadcasted_iota(jnp.int32, (LANES, LANES), 0) >> 6
    c = lax.broadcasted_iota(jnp.int32, (LANES, LANES), 1) >> 6
    return (r == c).astype(BF16)


def _split3(x):
    hi = x.astype(BF16)
    r1 = x - hi.astype(F32)
    mid = r1.astype(BF16)
    lo = (r1 - mid.astype(F32)).astype(BF16)
    return hi, mid, lo


def _exact_dot(x, ones_bf16, ones_first=False):
    out = None
    for piece in _split3(x):
        if ones_first:
            t = jnp.dot(ones_bf16, piece, preferred_element_type=F32)
        else:
            t = jnp.dot(piece, ones_bf16, preferred_element_type=F32)
        out = t if out is None else out + t
    return out


def _head_sum(x, bd):
    n = x.shape[1] // LANES
    parts = [_exact_dot(x[:, i * LANES:(i + 1) * LANES], bd) for i in range(n)]
    return parts[0] if n == 1 else jnp.concatenate(parts, axis=1)


def _dot_nt(a, b):
    return lax.dot_general(a, b, (((1,), (1,)), ((), ())), preferred_element_type=F32)


def _dot_tn(a, b):
    return lax.dot_general(a, b, (((0,), (0,)), ((), ())), preferred_element_type=F32)


def _colsum(x):
    return jnp.sum(x, axis=0, keepdims=True)


def _matmul_tn_acc(at, b, name, tk=512):
    m, k = at.shape
    n = b.shape[1]

    def body(a_ref, b_ref, o_ref):
        j = pl.program_id(0)

        @pl.when(j == 0)
        def _():
            o_ref[...] = jnp.zeros_like(o_ref)

        o_ref[...] += jnp.dot(a_ref[...], b_ref[...].astype(BF16), preferred_element_type=F32)

    return pl.pallas_call(
        body, name=name, grid=(k // tk,),
        in_specs=[pl.BlockSpec((m, tk), lambda j: (0, j)), pl.BlockSpec((tk, n), lambda j: (j, 0))],
        out_specs=pl.BlockSpec((m, n), lambda j: (0, 0)),
        out_shape=jax.ShapeDtypeStruct((m, n), F32), compiler_params=_params("arbitrary"),
    )(at, b)


def _inproj_bwd(du_a, du_b, du_g, w_a, w_b, w_g, x, dx2, g, exchange=None, tm=256):
    s, d = x.shape
    nb = s // tm

    def body(*refs):
        ((da_ref, db_ref, dg_ref, wa_ref, wb_ref, wg_ref, x_ref, dx2_ref, g_ref), (gx_ref, gg_ref), _,
         moves) = _split_refs(refs, 9, 2, exchange)
        i = pl.program_id(0)
        if moves:
            moves.start(also=(i == 0))

        @pl.when(i == 0)
        def _():
            gg_ref[...] = jnp.zeros_like(gg_ref)

        dh = _dot_nt(da_ref[...].astype(BF16), wa_ref[...])
        dh += _dot_nt(db_ref[...].astype(BF16), wb_ref[...])
        dh += _dot_nt(dg_ref[...].astype(BF16), wg_ref[...])
        xv = x_ref[...]
        r = lax.rsqrt(jnp.mean(xv * xv, axis=-1, keepdims=True) + RMS_EPS)
        xh = xv * r
        gg_ref[...] += _colsum(dh * xh)
        dxh = dh * g_ref[...]
        gx_ref[...] = dx2_ref[...] + r * (dxh - xh * jnp.mean(dxh * xh, axis=-1, keepdims=True))
        if moves:
            moves.wait(also=(i == nb - 1))

    row = lambda w: pl.BlockSpec((tm, w), lambda i: (i, 0))
    full = lambda a: pl.BlockSpec(a.shape, lambda i: (0, 0))
    ex_in = exchange.operands if exchange else []
    ex_out = exchange.out_shapes if exchange else []
    res = pl.pallas_call(
        body, name="inproj_bwd", grid=(nb,),
        in_specs=[row(SEC), row(SEC), row(GATE_COLS), full(w_a), full(w_b), full(w_g), row(d), row(d), full(g)]
                 + [ANY] * len(ex_in),
        out_specs=[row(d), pl.BlockSpec((1, d), lambda i: (0, 0))] + [ANY] * len(ex_out),
        out_shape=[jax.ShapeDtypeStruct((s, d), F32), jax.ShapeDtypeStruct((1, d), F32)] + ex_out,
        scratch_shapes=exchange.scratch() if exchange else [],
        compiler_params=_params("arbitrary"),
    )(du_a, du_b, du_g, w_a, w_b, w_g, x, dx2, g, *ex_in)
    return res[0], res[1], list(res[2:])


def _rwkv_elementwise(ua, prev_row, first, mu, wl, w0, a0, kkw, kaw, bd):
    tm = ua.shape[0]
    rows = lax.broadcasted_iota(jnp.int32, (tm, 1), 0)
    prev = jnp.where(first, jnp.zeros_like(prev_row), prev_row)
    shifted = jnp.where(rows == 0, prev, pltpu.roll(ua, 1, 0))
    delta = shifted - ua
    us = ua + delta * mu
    r = us[:, 0:512]
    k0 = us[:, 512:1024]
    v = us[:, 1024:1536]
    lo = us[:, 1536:1664]
    gate = us[:, 1664:2176]
    lane = lax.broadcasted_iota(jnp.int32, (1, LANES), 1)
    th = jnp.tanh(lo)
    lin = jnp.where(lane < LORA, th, lo)
    ll = jnp.dot(lin.astype(BF16), wl, preferred_element_type=F32)
    sz = _sigmoid(w0 + ll[:, :512])
    e = sz * math.exp(-0.5)
    dec = jnp.exp(-e)
    a = _sigmoid(a0 + ll[:, 512:])
    kk0 = k0 * kkw
    ss = _head_sum(kk0 * kk0, bd)
    nrm = jnp.maximum(jnp.sqrt(ss), 1e-12)
    kk = kk0 / nrm
    k = k0 * (1.0 + (a - 1.0) * kaw)
    return dict(delta=delta, us=us, r=r, k0=k0, v=v, lo=lo, gate=gate, th=th, lin=lin, sz=sz, e=e, dec=dec,
                a=a, kk0=kk0, ss=ss, nrm=nrm, kk=kk, k=k)


def _rwkv_front(x, g, w_a, mu, wl, w0, a0, kkw, kaw, tm=512):
    s, d = x.shape

    def body(x_ref, g_ref, wa_ref, mu_ref, wl_ref, w0_ref, a0_ref, kkw_ref, kaw_ref,
             h_ref, ua_ref, r_ref, w_ref, k_ref, v_ref, a_ref, b_ref, gate_ref, last_row):
        i = pl.program_id(0)
        xv = x_ref[...]
        h = (xv * lax.rsqrt(jnp.mean(xv * xv, axis=-1, keepdims=True) + RMS_EPS) * g_ref[...]).astype(BF16)
        h_ref[...] = h
        ua = jnp.dot(h, wa_ref[...], preferred_element_type=F32)
        ua_ref[...] = ua

        @pl.when(i == 0)
        def _():
            last_row[...] = jnp.zeros_like(last_row)

        f = _rwkv_elementwise(ua, last_row[...], i == 0, mu_ref[...], wl_ref[...], w0_ref[...],
                              a0_ref[...], kkw_ref[...], kaw_ref[...], _head_ones())
        last_row[...] = ua[tm - 1:tm, :]
        r_ref[...] = f["r"]
        w_ref[...] = f["dec"]
        k_ref[...] = f["k"]
        v_ref[...] = f["v"]
        a_ref[...] = -f["kk"]
        b_ref[...] = f["kk"] * f["a"]
        gate_ref[...] = f["gate"]

    vec = lambda w: pl.BlockSpec((1, w), lambda i: (0, 0))
    row = lambda w: pl.BlockSpec((tm, w), lambda i: (i, 0))
    return pl.pallas_call(
        body, name="rwkv_front", grid=(s // tm,),
        in_specs=[row(d), vec(d), pl.BlockSpec(w_a.shape, lambda i: (0, 0), pipeline_mode=pl.Buffered(1)),
                  vec(SEC), pl.BlockSpec((LANES, 2 * D_HALF), lambda i: (0, 0)),
                  vec(D_HALF), vec(D_HALF), vec(D_HALF), vec(D_HALF)],
        out_specs=[row(d), row(SEC)] + [row(D_HALF)] * 7,
        out_shape=[jax.ShapeDtypeStruct((s, d), BF16), jax.ShapeDtypeStruct((s, SEC), F32)]
                  + [jax.ShapeDtypeStruct((s, D_HALF), F32)] * 7,
        scratch_shapes=[pltpu.VMEM((1, SEC), F32)],
        compiler_params=_params("arbitrary"),
    )(x, g, w_a, mu, wl, w0, a0, kkw, kaw)


SCAN_TB = 128
N_PAIRS = 4


def _pair_sum(x, left):
    s_l = jnp.sum(jnp.where(left, x, 0.0), axis=1, keepdims=True)
    s_r = jnp.sum(jnp.where(left, 0.0, x), axis=1, keepdims=True)
    return jnp.where(left, s_l, s_r)


def _pair_dot(x, row_l, row_r, left):
    s_l = jnp.sum(x * row_l, axis=1, keepdims=True)
    s_r = jnp.sum(x * row_r, axis=1, keepdims=True)
    return jnp.where(left, s_l, s_r)


def _halves(rows8):
    lane = lax.broadcasted_iota(jnp.int32, rows8.shape, 1)
    keep_left = (lane & (LANES - 1)) < HEAD
    return jnp.where(keep_left, rows8, 0.0), jnp.where(keep_left, 0.0, rows8)


def _quad_consts():
    lane = lax.broadcasted_iota(jnp.int32, (HEAD, 2 * LANES), 1)
    rowi = lax.broadcasted_iota(jnp.int32, (HEAD, 2 * LANES), 0)
    diag2 = rowi == (lane & (HEAD - 1))
    r = lax.broadcasted_iota(jnp.int32, (2 * LANES, 2 * LANES), 0) >> 6
    c = lax.broadcasted_iota(jnp.int32, (2 * LANES, 2 * LANES), 1) >> 6
    return diag2, (r == c).astype(BF16)


def _rows_to_columns(x8, diag2, bd2):
    lhs = jnp.concatenate([jnp.where(diag2, x8[i:i + 1], 0.0).astype(BF16) for i in range(SUBLANES)], axis=0)
    return jnp.dot(lhs, bd2, preferred_element_type=F32)


def _diag_rows(qtile, diag2, bd2, sub_row2):
    res = jnp.dot(qtile, bd2, preferred_element_type=F32)
    out = jnp.zeros((SUBLANES, 2 * LANES), F32)
    for i in range(SUBLANES):
        out = jnp.where(sub_row2 == i, _colsum(jnp.where(diag2, res[i * HEAD:(i + 1) * HEAD], 0.0)), out)
    return out


def _store_tile(qbuf, slot, p, i, x):
    qbuf[slot, p // 2, i * HEAD:(i + 1) * HEAD, (p % 2) * LANES:(p % 2 + 1) * LANES] = x.astype(BF16)


def _left_half():
    return lax.broadcasted_iota(jnp.int32, (HEAD, LANES), 1) < HEAD


def _split_refs(refs, n_rows, n_out, exchange):
    n_in = len(exchange.operands) if exchange else 0
    n_ex_out = len(exchange.out_shapes) if exchange else 0
    refs = list(refs)
    rows, refs = refs[:n_rows], refs[n_rows:]
    ex_in, refs = refs[:n_in], refs[n_in:]
    outs, refs = refs[:n_out], refs[n_out:]
    ex_out, refs = refs[:n_ex_out], refs[n_ex_out:]
    scratch, sems = (refs[:-3], refs[-3:]) if exchange else (refs, None)
    moves = exchange.moves(ex_in, ex_out, sems) if exchange else None
    return rows, outs, scratch, moves


def _wkv_fwd(r, w, k, a, b, v, exchange=None):
    s = r.shape[0]
    tb = SCAN_TB
    nb = s // tb

    def body(*refs):
        (r_ref, w_ref, k_ref, a_ref, b_ref, v_ref), (y_ref, st_ref), (state, vbuf, qbuf), moves = _split_refs(
            refs, 6, 2, exchange)
        g = pl.program_id(0)
        if moves:
            moves.start(also=(g == 0))

        @pl.when(g == 0)
        def _():
            state[...] = jnp.zeros_like(state)
            qbuf[...] = jnp.zeros_like(qbuf)

        left = _left_half()
        diag2, bd2 = _quad_consts()
        sub_row2 = lax.broadcasted_iota(jnp.int32, (SUBLANES, 2 * LANES), 0)
        groups = tb // SUBLANES
        quads = [slice(g2 * 2 * LANES, (g2 + 1) * 2 * LANES) for g2 in range(2)]

        def rows_of(q):
            return pl.ds(pl.multiple_of(q * SUBLANES, SUBLANES), SUBLANES)

        def v_tiles(q, slot):
            v8 = v_ref[rows_of(q), :]
            for g2 in range(2):
                vbuf[slot, g2] = _rows_to_columns(v8[:, quads[g2]], diag2, bd2)

        def chain(q, slot):
            rows8 = rows_of(q)
            a8, w8, b8, k8, r8 = (x[rows8, :] for x in (a_ref, w_ref, b_ref, k_ref, r_ref))
            pairs = [slice(p * LANES, (p + 1) * LANES) for p in range(N_PAIRS)]
            a_next = pltpu.roll(a8, SUBLANES - 1, 0)
            (a8_l, a8_r), (wa8_l, wa8_r) = _halves(a8), _halves(w8 * a_next)
            ba8 =jnp.concatenate([_pair_sum(b8[:, pr] * a_next[:, pr], left[0:SUBLANES]) for pr in pairs], axis=1)
            ka8 = jnp.concatenate([_pair_sum(k8[:, pr] * a_next[:, pr], left[0:SUBLANES]) for pr in pairs], axis=1)
            sp = [state[p] for p in range(N_PAIRS)]
            for i in range(0, SUBLANES, 2):
                r0, r1 = slice(i, i + 1), slice(i + 1, i + 2)
                sums = [(_pair_dot(sp[p], a8_l[r0, pairs[p]], a8_r[r0, pairs[p]], left),
                         _pair_dot(sp[p], wa8_l[r0, pairs[p]], wa8_r[r0, pairs[p]], left)) for p in range(N_PAIRS)]
                sa0, sa1 = [s[0] for s in sums], [s[1] for s in sums]
                for p in range(N_PAIRS):
                    pr = pairs[p]
                    inner = slice((p % 2) * LANES, (p % 2 + 1) * LANES)
                    vt0 = vbuf[slot, p // 2, i * HEAD:(i + 1) * HEAD, inner]
                    vt1 = vbuf[slot, p // 2, (i + 1) * HEAD:(i + 2) * HEAD, inner]
                    sa_next = sa1[p] + sa0[p] * ba8[r0, pr] + vt0 * ka8[r0, pr]
                    s1 = sp[p] * w8[r0, pr] + sa0[p] * b8[r0, pr] + vt0 * k8[r0, pr]
                    st_ref[q * SUBLANES + i, p] = s1
                    _store_tile(qbuf, slot, p, i, s1 * r8[r0, pr])
                    s2 = s1 * w8[r1, pr] + sa_next * b8[r1, pr] + vt1 * k8[r1, pr]
                    st_ref[q * SUBLANES + i + 1, p] = s2
                    _store_tile(qbuf, slot, p, i + 1, s2 * r8[r1, pr])
                    sp[p] = s2
            for p in range(N_PAIRS):
                state[p] = sp[p]

        def y_rows(q, slot):
            for g2 in range(2):
                y_ref[rows_of(q), quads[g2]] = _diag_rows(qbuf[slot, g2], diag2, bd2, sub_row2)

        v_tiles(0, 0)

        def two_groups(j, carry):
            q0 = 2 * j
            v_tiles(q0 + 1, 1)
            chain(q0, 0)
            y_rows(jnp.maximum(q0 - 1, 0), 1)
            v_tiles(jnp.minimum(q0 + 2, groups - 1), 0)
            chain(q0 + 1, 1)
            y_rows(q0, 0)
            return carry

        lax.fori_loop(0, groups // 2, two_groups, 0)
        y_rows(groups - 1, 1)
        if moves:
            moves.wait(also=(g == nb - 1))

    rows = pl.BlockSpec((tb, D_HALF), lambda g: (g, 0))
    ex_in = exchange.operands if exchange else []
    ex_out = exchange.out_shapes if exchange else []
    res = pl.pallas_call(
        body, name="wkv_fwd", grid=(nb,),
        in_specs=[rows] * 6 + [ANY] * len(ex_in),
        out_specs=[rows, pl.BlockSpec((tb, N_PAIRS, HEAD, LANES), lambda g: (g, 0, 0, 0))] + [ANY] * len(ex_out),
        out_shape=[jax.ShapeDtypeStruct((s, D_HALF), F32),
                   jax.ShapeDtypeStruct((s, N_PAIRS, HEAD, LANES), F32)] + ex_out,
        scratch_shapes=[pltpu.VMEM((N_PAIRS, HEAD, LANES), F32),
                        pltpu.VMEM((2, 2, SUBLANES * HEAD, 2 * LANES), F32),
                        pltpu.VMEM((2, 2, SUBLANES * HEAD, 2 * LANES), BF16)]
                       + (exchange.scratch() if exchange else []),
        compiler_params=_params("arbitrary"),
    )(r, w, k, a, b, v, *ex_in)
    return res[0], res[1], list(res[2:])


def _wkv_bwd(r, w, k, a, b, v, dy, st, exchange=None):
    s = r.shape[0]
    tb = SCAN_TB
    nb = s // tb

    def body(*refs):
        ((r_ref, w_ref, k_ref, a_ref, b_ref, v_ref, dy_ref, st_ref, before_ref),
         (dr_ref, dw_ref, dk_ref, dv_ref, da_ref, db_ref), (dstate, vbuf, qbuf, sbuf),
         moves) = _split_refs(refs, 9, 6, exchange)
        g = pl.program_id(0)
        first_block = g == nb - 1
        if moves:
            moves.start(also=(g == 0))

        @pl.when(g == 0)
        def _():
            dstate[...] = jnp.zeros_like(dstate)
            qbuf[...] = jnp.zeros_like(qbuf)

        left = _left_half()
        diag2, bd2 = _quad_consts()
        sub_row = lax.broadcasted_iota(jnp.int32, (SUBLANES, LANES), 0)
        sub_row2 = lax.broadcasted_iota(jnp.int32, (SUBLANES, 2 * LANES), 0)
        groups = tb // SUBLANES
        quads = [slice(g2 * 2 * LANES, (g2 + 1) * 2 * LANES) for g2 in range(2)]
        row_refs = (dr_ref, dw_ref, dk_ref, da_ref, db_ref)

        def rows_of(q):
            return pl.ds(pl.multiple_of(q * SUBLANES, SUBLANES), SUBLANES)

        def state_before(q, i, p):
            if i > 0:
                return st_ref[q * SUBLANES + i - 1, p]
            return jnp.where(q == 0, jnp.where(first_block, 0.0, before_ref[0, p]),
                             st_ref[jnp.maximum(q * SUBLANES - 1, 0), p])

        def column_tiles(q, slot):
            rows8 = rows_of(q)
            for kind, ref in enumerate((v_ref, dy_ref)):
                x8 = ref[rows8, :]
                for g2 in range(2):
                    vbuf[slot, kind, g2] = _rows_to_columns(x8[:, quads[g2]], diag2, bd2)
            a8 = a_ref[rows8, :]
            for i in range(SUBLANES):
                for p in range(N_PAIRS):
                    _store_tile(sbuf, 0, p, i, state_before(q, i, p) * a8[i:i + 1, p * LANES:(p + 1) * LANES])
            for g2 in range(2):
                vbuf[slot, 2, g2] = jnp.dot(sbuf[0, g2], bd2, preferred_element_type=F32)

        def chain(q, slot):
            rows8 = rows_of(q)
            a8, w8, b8, k8, r8 = (x[rows8, :] for x in (a_ref, w_ref, b_ref, k_ref, r_ref))
            b8_l, b8_r = _halves(b8)
            dsp = [dstate[p] for p in range(N_PAIRS)]
            outs = [[jnp.zeros((SUBLANES, LANES), F32) for _ in row_refs] for _ in range(N_PAIRS)]
            after = [st_ref[q * SUBLANES + SUBLANES - 1, p] for p in range(N_PAIRS)]
            for i in reversed(range(SUBLANES)):
                row = slice(i, i + 1)
                pl_ = [slice(p * LANES, (p + 1) * LANES) for p in range(N_PAIRS)]
                tile = [(p // 2, slice(i * HEAD, (i + 1) * HEAD), slice((p % 2) * LANES, (p % 2 + 1) * LANES))
                        for p in range(N_PAIRS)]
                sp = [state_before(q, i, p) for p in range(N_PAIRS)]
                dyt = [vbuf[(slot, 1) + tile[p]] for p in range(N_PAIRS)]
                ds = [dsp[p] + dyt[p] * r8[row, pl_[p]] for p in range(N_PAIRS)]
                dsa = [_pair_dot(ds[p], b8_l[row, pl_[p]], b8_r[row, pl_[p]], left) for p in range(N_PAIRS)]
                sa = [vbuf[(slot, 2) + tile[p]] for p in range(N_PAIRS)]
                for p in range(N_PAIRS):
                    ar, wr, br, kr = (x[row, pl_[p]] for x in (a8, w8, b8, k8))
                    vt = vbuf[(slot, 0) + tile[p]]
                    dsp[p] = ds[p] * wr + dsa[p] * ar
                    new = (_colsum(after[p] * dyt[p]), _colsum(ds[p] * sp[p]), _colsum(ds[p] * vt),
                           _colsum(sp[p] * dsa[p]), _colsum(ds[p] * sa[p]))
                    outs[p] = [jnp.where(sub_row == i, n, o) for n, o in zip(new, outs[p])]
                    _store_tile(qbuf, slot, p, i, ds[p] * kr)
                after = sp
            for p in range(N_PAIRS):
                dstate[p] = dsp[p]
                for ref, o in zip(row_refs, outs[p]):
                    ref[rows8, p * LANES:(p + 1) * LANES] = o

        def dv_rows(q, slot):
            for g2 in range(2):
                dv_ref[rows_of(q), quads[g2]] = _diag_rows(qbuf[slot, g2], diag2, bd2, sub_row2)

        column_tiles(groups - 1, 0)

        def two_groups(j, carry):
            q0 = groups - 1 - 2 * j
            column_tiles(q0 - 1, 1)
            chain(q0, 0)
            dv_rows(jnp.minimum(q0 + 1, groups - 1), 1)
            column_tiles(jnp.maximum(q0 - 2, 0), 0)
            chain(q0 - 1, 1)
            dv_rows(q0, 0)
            return carry

        lax.fori_loop(0, groups // 2, two_groups, 0)
        dv_rows(0, 1)
        if moves:
            moves.wait(also=(g == nb - 1))

    rows = pl.BlockSpec((tb, D_HALF), lambda g: (nb - 1 - g, 0))
    ex_in = exchange.operands if exchange else []
    ex_out = exchange.out_shapes if exchange else []
    res = pl.pallas_call(
        body, name="wkv_bwd", grid=(nb,),
        in_specs=[rows] * 7 + [pl.BlockSpec((tb, N_PAIRS, HEAD, LANES), lambda g: (nb - 1 - g, 0, 0, 0)),
                               pl.BlockSpec((1, N_PAIRS, HEAD, LANES),
                                            lambda g: (jnp.maximum((nb - 1 - g) * tb - 1, 0), 0, 0, 0))]
                 + [ANY] * len(ex_in),
        out_specs=[rows] * 6 + [ANY] * len(ex_out),
        out_shape=[jax.ShapeDtypeStruct((s, D_HALF), F32)] * 6 + ex_out,
        scratch_shapes=[pltpu.VMEM((N_PAIRS, HEAD, LANES), F32),
                        pltpu.VMEM((2, 3, 2, SUBLANES * HEAD, 2 * LANES), F32),
                        pltpu.VMEM((2, 2, SUBLANES * HEAD, 2 * LANES), BF16),
                        pltpu.VMEM((1, 2, SUBLANES * HEAD, 2 * LANES), BF16)]
                       + (exchange.scratch() if exchange else []),
        compiler_params=_params("arbitrary"),
    )(r, w, k, a, b, v, dy, st, st, *ex_in)
    return list(res[:6]), list(res[6:])


def _rwkv_post_math(y, r, k, v, gate, lw, lb, rk, bd):
    mean = _head_sum(y, bd) * (1.0 / HEAD)
    yc = y - mean
    var = _head_sum(yc * yc, bd) * (1.0 / HEAD)
    rstd = lax.rsqrt(var + LNX_EPS)
    yn = yc * rstd
    rkk = _head_sum(r * k * rk, bd)
    sg = _sigmoid(gate)
    pre = yn * lw + lb + rkk * v
    return yn, rstd, rkk, sg, pre


def _rwkv_prep_bwd(u_a, h_t, grads, mu, wl, w0, a0, kkw, kaw, tm=256):
    s = u_a.shape[0]
    nb = s // tm
    d = h_t.shape[0]

    def body(ua_ref, prev_ref, ht_ref, drs_ref, dws_ref, dks_ref, dvs_ref, das_ref, dbs_ref, drb_ref, dkb_ref, dvb_ref,
             dgt_ref, mu_ref, wl_ref, w0_ref, a0_ref, kkw_ref, kaw_ref,
             du_ref, dwa_ref, dmu_ref, dwl_ref, dw0_ref, da0_ref, dkkw_ref, dkaw_ref, carry):
        i = pl.program_id(0)

        @pl.when(i == 0)
        def _():
            carry[...] = jnp.zeros_like(carry)
            for ref in (dwa_ref, dmu_ref, dwl_ref, dw0_ref, da0_ref, dkkw_ref, dkaw_ref):
                ref[...] = jnp.zeros_like(ref)

        bd = _head_ones()
        mu_v, wl_v, kkw_v, kaw_v = mu_ref[...], wl_ref[...], kkw_ref[...], kaw_ref[...]
        f = _rwkv_elementwise(ua_ref[...], prev_ref[7:8, :], i == nb - 1, mu_v, wl_v, w0_ref[...],
                              a0_ref[...], kkw_v, kaw_v, bd)
        a, kk, k0 = f["a"], f["kk"], f["k0"]
        dk = dks_ref[...] + dkb_ref[...]
        dbs = dbs_ref[...]
        dkk = dbs * a - das_ref[...]
        da = dbs * kk + dk * k0 * kaw_v
        dk0 = dk * (1.0 + (a - 1.0) * kaw_v)
        dkaw_ref[...] += _colsum(dk * k0 * (a - 1.0))
        inv = 1.0 / f["nrm"]
        proj = _head_sum(dkk * kk, bd)
        dkk0 = jnp.where(f["ss"] > 1e-24, (dkk - kk * proj) * inv, dkk * inv)
        dk0 = dk0 + dkk0 * kkw_v
        dkkw_ref[...] += _colsum(dkk0 * k0)
        dza = da * a * (1.0 - a)
        da0_ref[...] += _colsum(dza)
        dz = -dws_ref[...] * f["dec"] * f["e"] * (1.0 - f["sz"])
        dw0_ref[...] += _colsum(dz)
        dll = jnp.concatenate([dz, dza], axis=1).astype(BF16)
        dwl_ref[...] += _dot_tn(f["lin"].astype(BF16), dll)
        dlin = _dot_nt(dll, wl_v)
        lane = lax.broadcasted_iota(jnp.int32, (1, LANES), 1)
        th = f["th"]
        dlo = jnp.where(lane < LORA, dlin * (1.0 - th * th), dlin)
        dus = jnp.concatenate([drs_ref[...] + drb_ref[...], dk0, dvs_ref[...] + dvb_ref[...], dlo, dgt_ref[...]],
                              axis=1)
        dmu_ref[...] += _colsum(dus * f["delta"])
        g1 = dus * mu_v
        rows = lax.broadcasted_iota(jnp.int32, (tm, 1), 0)
        up = jnp.where(rows == tm - 1, carry[...], pltpu.roll(g1, tm - 1, 0))
        dua = dus - g1 + up
        du_ref[...] = dua
        dwa_ref[...] += jnp.dot(ht_ref[...], dua.astype(BF16), preferred_element_type=F32)
        carry[...] = g1[0:1, :]

    rev = lambda w: pl.BlockSpec((tm, w), lambda i: (nb - 1 - i, 0))
    vec = lambda w: pl.BlockSpec((1, w), lambda i: (0, 0))
    wl_spec = pl.BlockSpec((LANES, 2 * D_HALF), lambda i: (0, 0))
    return pl.pallas_call(
        body, name="rwkv_prep_bwd", grid=(nb,),
        in_specs=[rev(SEC), pl.BlockSpec((8, SEC), lambda i: (jnp.maximum((nb - 1 - i) * (tm // 8) - 1, 0), 0)),
                  pl.BlockSpec((d, tm), lambda i: (0, nb - 1 - i))]
                 + [rev(D_HALF)] * 10 + [vec(SEC), wl_spec] + [vec(D_HALF)] * 4,
        out_specs=[rev(SEC), pl.BlockSpec((d, SEC), lambda i: (0, 0)), vec(SEC), wl_spec] + [vec(D_HALF)] * 4,
        out_shape=[jax.ShapeDtypeStruct((s, SEC), F32), jax.ShapeDtypeStruct((d, SEC), F32),
                   jax.ShapeDtypeStruct((1, SEC), F32),
                   jax.ShapeDtypeStruct((LANES, 2 * D_HALF), F32)] + [jax.ShapeDtypeStruct((1, D_HALF), F32)] * 4,
        scratch_shapes=[pltpu.VMEM((1, SEC), F32)],
        compiler_params=_params("arbitrary"),
    )(u_a, u_a, h_t, *grads, mu, wl, w0, a0, kkw, kaw)


def _tri(tm, lower):
    r = lax.broadcasted_iota(jnp.int32, (tm, tm), 0)
    c = lax.broadcasted_iota(jnp.int32, (tm, tm), 1)
    return ((r >= c) if lower else (r <= c)).astype(BF16)


def _head_rms(x, g, bd):
    rinv = lax.rsqrt(_head_sum(x * x, bd) * (1.0 / HEAD) + RMS_EPS)
    xh = x * rinv
    return xh, rinv, xh * g


def _fox_front(h, w_b, fb, qg, kg, tm=512):
    s, d = h.shape

    def body(h_ref, wb_ref, fb_ref, qg_ref, kg_ref, ub_ref, q_ref, k_ref, v_ref, cc_ref, cr_ref, carry):
        i = pl.program_id(0)

        @pl.when(i == 0)
        def _():
            carry[...] = jnp.zeros_like(carry)

        ub_ref[...] = jnp.dot(h_ref[...], wb_ref[...], preferred_element_type=F32)
        bd = _head_ones()
        _, _, qn = _head_rms(ub_ref[:, 0:512], qg_ref[...], bd)
        _, _, kn = _head_rms(ub_ref[:, 512:1024], kg_ref[...], bd)
        q_ref[...] = (qn * ATT_SCALE).astype(BF16)
        k_ref[...] = kn.astype(BF16)
        v_ref[...] = ub_ref[:, 1024:1536].astype(BF16)
        lane = lax.broadcasted_iota(jnp.int32, (1, LANES), 1)
        logf = jnp.where(lane < N_HEADS, _log_sigmoid(ub_ref[:, 2048:2176] + fb_ref[...]), 0.0)
        cum = _exact_dot(logf, _tri(tm, True), ones_first=True) + carry[...]
        for h in range(N_HEADS):
            cc_ref[h] = jnp.broadcast_to(cum[:, h:h + 1], (tm, LANES))
        cr_ref[...] = jnp.transpose(cum)[0:N_HEADS, :]
        carry[...] = cum[tm - 1:tm, :]

    blk = pl.BlockSpec((tm, D_HALF), lambda i: (i, 0))
    return pl.pallas_call(
        body, name="fox_front", grid=(s // tm,),
        in_specs=[pl.BlockSpec((tm, d), lambda i: (i, 0)),
                  pl.BlockSpec(w_b.shape, lambda i: (0, 0), pipeline_mode=pl.Buffered(1)),
                  pl.BlockSpec((1, LANES), lambda i: (0, 0)),
                  pl.BlockSpec((1, D_HALF), lambda i: (0, 0)), pl.BlockSpec((1, D_HALF), lambda i: (0, 0))],
        out_specs=[pl.BlockSpec((tm, SEC), lambda i: (i, 0)), blk, blk, blk,
                   pl.BlockSpec((N_HEADS, tm, LANES), lambda i: (0, i, 0)), pl.BlockSpec((N_HEADS, tm), lambda i: (0, i))],
        out_shape=[jax.ShapeDtypeStruct((s, SEC), F32)] + [jax.ShapeDtypeStruct((s, D_HALF), BF16)] * 3
                  + [jax.ShapeDtypeStruct((N_HEADS, s, LANES), F32), jax.ShapeDtypeStruct((N_HEADS, s), F32)],
        scratch_shapes=[pltpu.VMEM((1, LANES), F32)],
        compiler_params=_params("arbitrary"),
    )(h, w_b, fb, qg, kg)


ATT_T = 256


def _tiles(nblk, by_query):
    if by_query:
        pairs = [(i, j) for i in range(nblk) for j in range(i + 1)]
    else:
        pairs = [(i, j) for j in range(nblk) for i in range(j, nblk)]
    return (jnp.asarray([p[0] for p in pairs], jnp.int32), jnp.asarray([p[1] for p in pairs], jnp.int32))


def _attn_fwd(q, k, v, cc, cr):
    s = q.shape[0]
    t = ATT_T
    nblk = s // t

    def body(qi_ref, kj_ref, q_ref, k_ref, v_ref, cc_ref, cr_ref, o_ref, lse_ref, m_sc, l_sc, acc_sc):
        i = qi_ref[pl.program_id(0)]
        j = kj_ref[pl.program_id(0)]

        @pl.when(j == 0)
        def _():
            m_sc[...] = jnp.full_like(m_sc, NEG)
            l_sc[...] = jnp.zeros_like(l_sc)
            acc_sc[...] = jnp.zeros_like(acc_sc)

        def tile(on_diagonal):
            causal = _causal_tile(t) if on_diagonal else None
            left = lax.broadcasted_iota(jnp.int32, (1, LANES), 1) < HEAD
            for p in range(N_PAIRS):
                lanes = slice(p * LANES, (p + 1) * LANES)
                q2, k2, v2 = q_ref[:, lanes], k_ref[:, lanes], v_ref[:, lanes]
                acc2 = acc_sc[:, lanes]
                for e in range(2):
                    h = 2 * p + e
                    msk = left if e == 0 else jnp.logical_not(left)
                    sc = _dot_nt(jnp.where(msk, q2, jnp.zeros_like(q2)), k2)
                    sc = sc + (_wide(cc_ref[h]) - cr_ref[h:h + 1, :])
                    if on_diagonal:
                        sc = jnp.where(causal, sc, NEG)
                    m_prev = m_sc[h]
                    m_new = jnp.maximum(m_prev, jnp.max(sc, axis=1, keepdims=True))
                    alpha = jnp.exp(m_prev - m_new)
                    pm = jnp.exp(sc - _wide(m_new))
                    l_sc[h] = alpha * l_sc[h] + jnp.sum(pm, axis=1, keepdims=True)
                    m_sc[h] = m_new
                    pv = jnp.dot(pm.astype(BF16), v2, preferred_element_type=F32)
                    acc2 = jnp.where(msk, alpha * acc2 + pv, acc2)
                acc_sc[:, lanes] = acc2

        pl.when(j < i)(functools.partial(tile, False))
        pl.when(j == i)(functools.partial(tile, True))

        @pl.when(j == i)
        def _():
            left = lax.broadcasted_iota(jnp.int32, (1, LANES), 1) < HEAD
            for p in range(N_PAIRS):
                lanes = slice(p * LANES, (p + 1) * LANES)
                inv = jnp.where(left, 1.0 / l_sc[2 * p], 1.0 / l_sc[2 * p + 1])
                o_ref[:, lanes] = acc_sc[:, lanes] * inv
            for h in range(N_HEADS):
                lse_ref[h] = m_sc[h] + jnp.log(l_sc[h])

    qi, kj = _tiles(nblk, by_query=True)
    qblk = pl.BlockSpec((t, D_HALF), lambda n, qi, kj: (qi[n], 0))
    kblk = pl.BlockSpec((t, D_HALF), lambda n, qi, kj: (kj[n], 0))
    qrep = pl.BlockSpec((N_HEADS, t, LANES), lambda n, qi, kj: (0, qi[n], 0))
    return pl.pallas_call(
        body, name="fox_attn_fwd",
        grid_spec=pltpu.PrefetchScalarGridSpec(
            num_scalar_prefetch=2, grid=(qi.shape[0],),
            in_specs=[qblk, kblk, kblk, qrep, pl.BlockSpec((N_HEADS, t), lambda n, qi, kj: (0, kj[n]))],
            out_specs=[qblk, qrep],
            scratch_shapes=[pltpu.VMEM((N_HEADS, t, LANES), F32), pltpu.VMEM((N_HEADS, t, LANES), F32),
                            pltpu.VMEM((t, D_HALF), F32)]),
        out_shape=[jax.ShapeDtypeStruct((s, D_HALF), F32), jax.ShapeDtypeStruct((N_HEADS, s, LANES), F32)],
        compiler_params=_params("arbitrary"),
    )(qi, kj, q, k, v, cc, cr)


def _causal_tile(t):
    return lax.broadcasted_iota(jnp.int32, (t, t), 0) >= lax.broadcasted_iota(jnp.int32, (t, t), 1)


def _wide(x):
    return jnp.concatenate([x, x], axis=1)


def _attn_probs(q2, k2, v2, do2, msk, causal, bias, lse_rows):
    zero = jnp.zeros_like(q2)
    qh = jnp.where(msk, q2, zero)
    doh = jnp.where(msk, do2, zero)
    sc = _dot_nt(qh, k2) + bias
    if causal is not None:
        sc = jnp.where(causal, sc, NEG)
    pm = jnp.exp(sc - _wide(lse_rows))
    dp = _dot_nt(doh, v2)
    return qh, doh, pm, dp


def _attn_bwd_rowdot(q, k, v, do, lse, cc, cr):
    s = q.shape[0]
    t = ATT_T
    nblk = s // t

    def body(qi_ref, kj_ref, q_ref, k_ref, v_ref, do_ref, lse_ref, cc_ref, cr_ref, dd_ref, acc):
        i = qi_ref[pl.program_id(0)]
        j = kj_ref[pl.program_id(0)]

        @pl.when(j == 0)
        def _():
            acc[...] = jnp.zeros_like(acc)

        def tile(on_diagonal):
            causal = _causal_tile(t) if on_diagonal else None
            left = lax.broadcasted_iota(jnp.int32, (1, LANES), 1) < HEAD
            for p in range(N_PAIRS):
                lanes = slice(p * LANES, (p + 1) * LANES)
                q2, k2, v2, do2 = q_ref[:, lanes], k_ref[:, lanes], v_ref[:, lanes], do_ref[:, lanes]
                for e in range(2):
                    h = 2 * p + e
                    msk = left if e == 0 else jnp.logical_not(left)
                    bias = _wide(cc_ref[h]) - cr_ref[h:h + 1, :]
                    _, _, pm, dp = _attn_probs(q2, k2, v2, do2, msk, causal, bias, lse_ref[h])
                    acc[h] += jnp.sum(pm * dp, axis=1, keepdims=True)

        pl.when(j < i)(functools.partial(tile, False))
        pl.when(j == i)(functools.partial(tile, True))

        @pl.when(j == i)
        def _():
            dd_ref[...] = acc[...]

    qi, kj = _tiles(nblk, by_query=True)
    qblk = pl.BlockSpec((t, D_HALF), lambda n, qi, kj: (qi[n], 0))
    qcol = pl.BlockSpec((N_HEADS, t, LANES), lambda n, qi, kj: (0, qi[n], 0))
    kblk = pl.BlockSpec((t, D_HALF), lambda n, qi, kj: (kj[n], 0))
    return pl.pallas_call(
        body, name="fox_attn_rowdot",
        grid_spec=pltpu.PrefetchScalarGridSpec(
            num_scalar_prefetch=2, grid=(qi.shape[0],),
            in_specs=[qblk, kblk, kblk, qblk, qcol, qcol, pl.BlockSpec((N_HEADS, t), lambda n, qi, kj: (0, kj[n]))],
            out_specs=qcol, scratch_shapes=[pltpu.VMEM((N_HEADS, t, LANES), F32)]),
        out_shape=jax.ShapeDtypeStruct((N_HEADS, s, LANES), F32),
        compiler_params=_params("arbitrary"),
    )(qi, kj, q, k, v, do, lse, cc, cr)


def _attn_bwd(q, k, v, do, lse, dd, cc, cr):
    s = q.shape[0]
    t = ATT_T
    nblk = s // t

    def body(qi_ref, kj_ref, q_ref, k_ref, v_ref, do_ref, lse_ref, dd_ref, cc_ref, cr_ref,
             dq_ref, dk_ref, dv_ref, dcr_ref, dk_sc, dv_sc, dcr_sc):
        i = qi_ref[pl.program_id(0)]
        j = kj_ref[pl.program_id(0)]

        @pl.when(pl.program_id(0) == 0)
        def _():
            dq_ref[...] = jnp.zeros_like(dq_ref)

        @pl.when(i == j)
        def _():
            dk_sc[...] = jnp.zeros_like(dk_sc)
            dv_sc[...] = jnp.zeros_like(dv_sc)
            dcr_sc[...] = jnp.zeros_like(dcr_sc)

        def tile(on_diagonal):
            causal = _causal_tile(t) if on_diagonal else None
            left = lax.broadcasted_iota(jnp.int32, (1, LANES), 1) < HEAD
            qrows = pl.ds(pl.multiple_of(i * t, t), t)
            for p in range(N_PAIRS):
                lanes = slice(p * LANES, (p + 1) * LANES)
                q2, k2, v2, do2 = q_ref[:, lanes], k_ref[:, lanes], v_ref[:, lanes], do_ref[:, lanes]
                zero = jnp.zeros_like(q2)
                dq2 = jnp.zeros((t, LANES), F32)
                dk2 = jnp.zeros((t, LANES), F32)
                dv2 = jnp.zeros((t, LANES), F32)
                for e in range(2):
                    h = 2 * p + e
                    msk = left if e == 0 else jnp.logical_not(left)
                    bias = _wide(cc_ref[h]) - cr_ref[h:h + 1, :]
                    qh, doh, pm, dp = _attn_probs(q2, k2, v2, do2, msk, causal, bias, lse_ref[h])
                    dsc = pm * (dp - _wide(dd_ref[h]))
                    dsb = dsc.astype(BF16)
                    dv2 += _dot_tn(pm.astype(BF16), doh)
                    dk2 += _dot_tn(dsb, qh)
                    dq2 += jnp.dot(dsb, jnp.where(msk, k2, zero), preferred_element_type=F32)
                    dcr_sc[h:h + 1, :] += -_colsum(dsc)
                dq_ref[qrows, lanes] += dq2 * ATT_SCALE
                dk_sc[:, lanes] += dk2
                dv_sc[:, lanes] += dv2

        pl.when(i > j)(functools.partial(tile, False))
        pl.when(i == j)(functools.partial(tile, True))

        @pl.when(i == nblk - 1)
        def _():
            dk_ref[...] = dk_sc[...]
            dv_ref[...] = dv_sc[...]
            dcr_ref[...] = dcr_sc[...]

    qi, kj = _tiles(nblk, by_query=False)
    qblk = pl.BlockSpec((t, D_HALF), lambda n, qi, kj: (qi[n], 0))
    qcol = pl.BlockSpec((N_HEADS, t, LANES), lambda n, qi, kj: (0, qi[n], 0))
    kblk = pl.BlockSpec((t, D_HALF), lambda n, qi, kj: (kj[n], 0))
    krow = pl.BlockSpec((N_HEADS, t), lambda n, qi, kj: (0, kj[n]))
    return pl.pallas_call(
        body, name="fox_attn_bwd",
        grid_spec=pltpu.PrefetchScalarGridSpec(
            num_scalar_prefetch=2, grid=(qi.shape[0],),
            in_specs=[qblk, kblk, kblk, qblk, qcol, qcol, qcol, krow],
            out_specs=[pl.BlockSpec((s, D_HALF), lambda n, qi, kj: (0, 0)), kblk, kblk, krow],
            scratch_shapes=[pltpu.VMEM((t, D_HALF), F32), pltpu.VMEM((t, D_HALF), F32), pltpu.VMEM((N_HEADS, t), F32)]),
        out_shape=[jax.ShapeDtypeStruct((s, D_HALF), F32)] * 3 + [jax.ShapeDtypeStruct((N_HEADS, s), F32)],
        compiler_params=_params("arbitrary"),
    )(qi, kj, q, k, v, do, lse, dd, cc, cr)


def _fox_prep_bwd(u_b, h_t, dq, dk, dv, dgate, dcum, fb, qg, kg, tm=256):
    s = u_b.shape[0]
    nb = s // tm
    d = h_t.shape[0]

    def body(ub_ref, ht_ref, dq_ref, dk_ref, dv_ref, dg_ref, dc_ref, fb_ref, qg_ref, kg_ref,
             du_ref, dwb_ref, dqg_ref, dkg_ref, dfb_ref, carry):
        i = pl.program_id(0)

        @pl.when(i == 0)
        def _():
            carry[...] = jnp.zeros_like(carry)
            dwb_ref[...] = jnp.zeros_like(dwb_ref)
            dqg_ref[...] = jnp.zeros_like(dqg_ref)
            dkg_ref[...] = jnp.zeros_like(dkg_ref)
            dfb_ref[...] = jnp.zeros_like(dfb_ref)

        bd = _head_ones()
        for lo, g_ref, d_ref, dgain_ref in ((0, qg_ref, dq_ref, dqg_ref), (512, kg_ref, dk_ref, dkg_ref)):
            gain = g_ref[...]
            xh, rinv, _ = _head_rms(ub_ref[:, lo:lo + 512], gain, bd)
            dn = d_ref[...]
            dgain_ref[...] += _colsum(dn * xh)
            dxh = dn * gain
            du_ref[:, lo:lo + 512] = rinv * (dxh - xh * (_head_sum(dxh * xh, bd) * (1.0 / HEAD)))
        du_ref[:, 1024:1536] = dv_ref[...]
        du_ref[:, 1536:2048] = dg_ref[...]
        lane = lax.broadcasted_iota(jnp.int32, (1, LANES), 1)
        dc = dc_ref[...]
        dlogf = _exact_dot(dc, _tri(tm, False), ones_first=True) + carry[...]
        carry[...] += _colsum(dc)
        fl = ub_ref[:, 2048:2176] + fb_ref[...]
        dfl = jnp.where(lane < N_HEADS, dlogf * (1.0 - _sigmoid(fl)), 0.0)
        du_ref[:, 2048:2176] = dfl
        dfb_ref[...] += _colsum(dfl)
        dwb_ref[...] += jnp.dot(ht_ref[...], du_ref[...].astype(BF16), preferred_element_type=F32)

    rev = lambda w: pl.BlockSpec((tm, w), lambda i: (nb - 1 - i, 0))
    vec = lambda w: pl.BlockSpec((1, w), lambda i: (0, 0))
    return pl.pallas_call(
        body, name="fox_prep_bwd", grid=(nb,),
        in_specs=[rev(SEC), pl.BlockSpec((d, tm), lambda i: (0, nb - 1 - i))] + [rev(D_HALF)] * 4
                 + [rev(LANES), vec(LANES), vec(D_HALF), vec(D_HALF)],
        out_specs=[rev(SEC), pl.BlockSpec((d, SEC), lambda i: (0, 0)), vec(D_HALF), vec(D_HALF), vec(LANES)],
        out_shape=[jax.ShapeDtypeStruct((s, SEC), F32), jax.ShapeDtypeStruct((d, SEC), F32),
                   jax.ShapeDtypeStruct((1, D_HALF), F32), jax.ShapeDtypeStruct((1, D_HALF), F32),
                   jax.ShapeDtypeStruct((1, LANES), F32)],
        scratch_shapes=[pltpu.VMEM((1, LANES), F32)],
        compiler_params=_params("arbitrary"),
    )(u_b, h_t, dq, dk, dv, dgate, dcum, fb, qg, kg)


def _merge(y, r, k, v, gate_a, o, u_b, h, x, tgt, w_g, wa, wb, wo, fg, lw, lb, rk, tm=256):
    s, d = x.shape

    def body(y_ref, r_ref, k_ref, v_ref, ga_ref, o_ref, gb_ref, h_ref, x_ref, t_ref, wg_ref, wa_ref, wb_ref, wo_ref,
             fg_ref, lw_ref, lb_ref, rk_ref,
             dx2_ref, dy_ref, drb_ref, dkb_ref, dvb_ref, dga_ref, do_ref, dgb_ref, dug_ref,
             dwa_ref, dwb_ref, dwo_ref, dfg_ref, loss_ref, dlw_ref, dlb_ref, drk_ref):
        i = pl.program_id(0)

        @pl.when(i == 0)
        def _():
            for ref in (dwa_ref, dwb_ref, dwo_ref, dfg_ref, loss_ref, dlw_ref, dlb_ref, drk_ref):
                ref[...] = jnp.zeros_like(ref)

        bd = _head_ones()
        wa_v, wb_v, wo_v, fg_v = wa_ref[...], wb_ref[...], wo_ref[...], fg_ref[...]
        rv, kv, vv, ga, lw_v, rk_v = r_ref[...], k_ref[...], v_ref[...], ga_ref[...], lw_ref[...], rk_ref[...]
        yn, rstd, rkk, sga, pre = _rwkv_post_math(y_ref[...], rv, kv, vv, ga, lw_v, lb_ref[...], rk_v, bd)
        silu_a = ga * sga
        gb, ov = gb_ref[...], o_ref[...]
        sgb = _sigmoid(gb)
        silu_b = gb * sgb
        ma = (pre * silu_a).astype(BF16)
        mb = (ov * silu_b).astype(BF16)
        ya = jnp.dot(ma, wa_v, preferred_element_type=F32)
        yb = jnp.dot(mb, wb_v, preferred_element_type=F32)
        ug = jnp.dot(h_ref[...], wg_ref[...], preferred_element_type=F32)
        sa = _sigmoid(ug[:, 0:d])
        sb = _sigmoid(ug[:, d:2 * d])
        merged = (sa * ya + sb * yb).astype(BF16)
        x2 = x_ref[...] + jnp.dot(merged, wo_v, preferred_element_type=F32)
        r2 = lax.rsqrt(jnp.mean(x2 * x2, axis=-1, keepdims=True) + RMS_EPS)
        x2h = x2 * r2
        err = x2h * fg_v - t_ref[...]
        loss_ref[...] += _colsum(err * err)
        dyo = err * (1.0 / d)
        dfg_ref[...] += _colsum(dyo * x2h)
        dx2h = dyo * fg_v
        dx2 = r2 * (dx2h - x2h * jnp.mean(dx2h * x2h, axis=-1, keepdims=True))
        dx2_ref[...] = dx2
        dx2b = dx2.astype(BF16)
        dmerged = _dot_nt(dx2b, wo_v)
        dwo_ref[...] += _dot_tn(merged, dx2b)
        dya = dmerged * sa
        dyb = dmerged * sb
        dug_ref[:, 0:d] = dya * ya * (1.0 - sa)
        dug_ref[:, d:2 * d] = dyb * yb * (1.0 - sb)
        dyab = dya.astype(BF16)
        dybb = dyb.astype(BF16)
        dwa_ref[...] += _dot_tn(ma, dyab)
        dwb_ref[...] += _dot_tn(mb, dybb)
        dmb = _dot_nt(dybb, wb_v)
        do_ref[...] = (dmb * silu_b).astype(BF16)
        dgb_ref[...] = dmb * ov * (sgb * (1.0 + gb * (1.0 - sgb)))
        dma = _dot_nt(dyab, wa_v)
        dga_ref[...] = dma * pre * (sga * (1.0 + ga * (1.0 - sga)))
        dpre = dma * silu_a
        dlw_ref[...] += _colsum(dpre * yn)
        dlb_ref[...] += _colsum(dpre)
        dyn = dpre * lw_v
        m1 = _head_sum(dyn, bd) * (1.0 / HEAD)
        m2 = _head_sum(dyn * yn, bd) * (1.0 / HEAD)
        dy_ref[...] = rstd * (dyn - m1 - yn * m2)
        dvb_ref[...] = dpre * rkk
        drkk = _head_sum(dpre * vv, bd)
        drb_ref[...] = drkk * kv * rk_v
        dkb_ref[...] = drkk * rv * rk_v
        drk_ref[...] += _colsum(drkk * rv * kv)

    row = lambda w: pl.BlockSpec((tm, w), lambda i: (i, 0))
    full = lambda a: pl.BlockSpec(a.shape, lambda i: (0, 0))
    once = lambda a: pl.BlockSpec(a.shape, lambda i: (0, 0), pipeline_mode=pl.Buffered(1))
    half = jax.ShapeDtypeStruct((s, D_HALF), F32)
    fshape = lambda a: jax.ShapeDtypeStruct(a.shape, F32)
    return pl.pallas_call(
        body, name="merge_fwd_bwd", grid=(s // tm,),
        in_specs=[row(D_HALF)] * 6 + [pl.BlockSpec((tm, D_HALF), lambda i: (i, 3)), row(d), row(d), row(d),
                                      once(w_g), once(wa), once(wb), once(wo), full(fg), full(lw), full(lb), full(rk)],
        out_specs=[row(d)] + [row(D_HALF)] * 7 + [row(GATE_COLS), full(wa), full(wb), full(wo), full(fg), full(fg),
                                                   full(lw), full(lb), full(rk)],
        out_shape=[jax.ShapeDtypeStruct((s, d), F32)] + [half] * 5 + [jax.ShapeDtypeStruct((s, D_HALF), BF16), half,
                                                                    jax.ShapeDtypeStruct((s, GATE_COLS), F32),
                                                                    fshape(wa), fshape(wb), fshape(wo), fshape(fg),
                                                                    fshape(fg), fshape(lw), fshape(lb), fshape(rk)],
        compiler_params=_params("arbitrary"),
    )(y, r, k, v, gate_a, o, u_b, h, x, tgt, w_g, wa, wb, wo, fg, lw, lb, rk)


def _lora_weight(w_up, a_up):
    z = jnp.zeros((LORA, D_HALF), w_up.dtype)
    return jnp.concatenate([jnp.concatenate([w_up, z], axis=1), jnp.concatenate([z, a_up], axis=1)], axis=0)


def _device_grads(x, tgt, p, w_a, w_up, a_up, late_weights, fwd_exchange=None, bwd_exchange=None, tail_exchange=None):
    wl = _lora_weight(w_up, a_up)
    rk = p["r_k"].reshape(1, D_HALF)
    fb = jnp.pad(p["f_bias"], ((0, 0), (0, LANES - N_HEADS)))
    qg = jnp.tile(p["q_norm_g"], (1, N_HEADS))
    kg = jnp.tile(p["k_norm_g"], (1, N_HEADS))
    fg = p["final_norm_g"].reshape(1, D_MODEL)
    mixer = (p["shift_mu"], wl, p["w0"], p["a0"], p["k_k"], p["k_a"])

    h, u_a, r, dec, k, v, av, bv, gate_a = _rwkv_front(x, p["norm_g"], w_a, *mixer)
    y, st, arrived = _wkv_fwd(r, dec, k, av, bv, v, fwd_exchange)

    w_b, w_g, w_out_a, w_out_b, w_out = late_weights(arrived)
    u_b, q, kn, vb, cc, cr = _fox_front(h, w_b, fb, qg, kg)
    o, lse = _attn_fwd(q, kn, vb, cc, cr)

    (dx2, dy, dr_b, dk_b, dv_b, dgate_a, do, dgate_b, du_g, dwa, dwb, dwo, dfg, loss_vec, dlw, dlb, drk) = _merge(
        y, r, k, v, gate_a, o, u_b, h, x, tgt, w_g, w_out_a, w_out_b, w_out, fg, p["lnx_w"], p["lnx_b"], rk)

    dd = _attn_bwd_rowdot(q, kn, vb, do, lse, cc, cr)
    dq, dk_att, dv_att, dcr = _attn_bwd(q, kn, vb, do, lse, dd, cc, cr)
    dcum = jnp.pad(dcr.T, ((0, 0), (0, LANES - N_HEADS)))
    h_t = h.T
    du_b, dw_b, dqg, dkg, dfb = _fox_prep_bwd(u_b, h_t, dq, dk_att, dv_att, dgate_b, dcum, fb, qg, kg)
    dw_g = _matmul_tn_acc(h_t, du_g, "dw_gate")

    scan_grads, sent = _wkv_bwd(r, dec, k, av, bv, v, dy, st,
                                bwd_exchange(dw_b, dw_g, dwa, dwb, dwo) if bwd_exchange else None)
    du_a, dw_a, dmu, dwl, dw0, da0, dkkw, dkaw = _rwkv_prep_bwd(
        u_a, h_t, (*scan_grads, dr_b, dk_b, dv_b, dgate_a), *mixer)
    dw_up, da_up = dwl[:LORA, :D_HALF], dwl[LORA:, D_HALF:]
    sent_last = _run_on_sequencer(tail_exchange(dw_a, dw_up, da_up), "scatter_tail", 1) if tail_exchange else []
    grad_x, dnorm_g, _ = _inproj_bwd(du_a, du_b, du_g, w_a, w_b, w_g, x, dx2, p["norm_g"])

    grads = dict(
        norm_g=dnorm_g, w_in=(dw_a, dw_b, dw_g), shift_mu=dmu,
        w_lora_up=dw_up, w0=dw0, a_lora_up=da_up, a0=da0, k_k=dkkw, k_a=dkaw,
        r_k=drk.reshape(1, N_HEADS, HEAD), lnx_w=dlw, lnx_b=dlb, f_bias=dfb[:, :N_HEADS],
        q_norm_g=dqg.reshape(N_HEADS, HEAD).sum(axis=0, keepdims=True),
        k_norm_g=dkg.reshape(N_HEADS, HEAD).sum(axis=0, keepdims=True),
        w_out_a=dwa, w_out_b=dwb, w_out=dwo, final_norm_g=dfg.reshape(D_MODEL))
    return loss_vec, grad_x, grads, sent, sent_last


CHIP_FLIPS = ((1, 0), (0, 1), (1, 1))
ANY = pl.BlockSpec(memory_space=pl.ANY)


def _position():
    return lax.axis_index("x"), lax.axis_index("y"), lax.axis_index("c")


def _flip(v, f):
    return 1 - v if f else v


def _both(a, b):
    if a is None:
        return b
    return a if b is None else jnp.logical_and(a, b)


def _when(cond, fn):
    if cond is None:
        fn()
    else:
        pl.when(cond)(fn)


class _Moves:
    def __init__(self, send_sems, recv_sems, local_sems):
        self.send_sems, self.recv_sems, self.local_sems = send_sems, recv_sems, local_sems
        self.remote, self.local = [], []

    def send(self, src, dst, peer, landing, send_if=None, recv_if=None, first=False):
        k = len(self.remote)
        sems = dict(send_sem=self.send_sems.at[k], recv_sem=self.recv_sems.at[k], device_id=peer, device_id_type=MESH)
        out = pltpu.make_async_remote_copy(src_ref=src, dst_ref=dst, **sems)
        arrival = pltpu.make_async_remote_copy(src_ref=src, dst_ref=landing, **sems)
        self.remote.append((out, arrival, send_if, recv_if, first))

    def copy(self, src, dst, cond=None):
        cp = pltpu.make_async_copy(src, dst, self.local_sems.at[len(self.local)])
        self.local.append((cp, cond))

    def start(self, also=None):
        for cp, cond in self.local:
            _when(_both(also, cond), cp.start)
        for out, _, send_if, _, _ in self.remote:
            _when(_both(also, send_if), out.start)

    def wait_arrivals(self, also=None, first=None):
        for _, arrival, _, recv_if, is_first in self.remote:
            if first is None or first == is_first:
                _when(_both(also, recv_if), arrival.wait_recv)

    def wait_sent(self, also=None):
        for out, _, send_if, _, _ in self.remote:
            _when(_both(also, send_if), out.wait_send)
        for cp, cond in self.local:
            _when(_both(also, cond), cp.wait)

    def wait(self, also=None):
        self.wait_arrivals(also)
        self.wait_sent(also)


class _Exchange:
    def __init__(self, operands, out_shapes, n_remote, n_local, build, relays=None, in_place=(), n_staging=0):
        self.operands, self.out_shapes = list(operands), list(out_shapes)
        self.n_remote, self.n_local, self.build = n_remote, n_local, build
        self.relays, self.in_place = relays, in_place
        self.n_staging = n_staging

    def scratch(self):
        return [pltpu.SemaphoreType.DMA((self.n_remote,)), pltpu.SemaphoreType.DMA((self.n_remote,)),
                pltpu.SemaphoreType.DMA((max(self.n_local, 1),))]

    def moves(self, in_refs, out_refs, sems):
        mv = _Moves(*sems)
        self.build(mv, in_refs, out_refs)
        return mv


def _run_on_sequencer(exchange, name, collective_id):
    ins = [jax.new_ref(a, memory_space=pltpu.MemorySpace.HBM) for a in exchange.operands]
    outs = [ins[i] if i in exchange.in_place else jax.empty_ref(s, memory_space=pltpu.MemorySpace.HBM)
            for i, s in enumerate(exchange.out_shapes)]
    forward, to_sibling = exchange.relays or (None, None)
    relay_scratch = [pltpu.SemaphoreType.DMA((stage[0],)) for stage in (forward, to_sibling) if stage for _ in range(2)]

    def launch(*sems):
        x, y, c = _position()
        peers = [(_flip(x, fx), _flip(y, fy), c) for fx, fy in CHIP_FLIPS] + ([(x, y, 1 - c)] if to_sibling else [])
        barrier = pltpu.get_barrier_semaphore()
        for peer in peers:
            pl.semaphore_signal(barrier, inc=1, device_id=peer, device_id_type=MESH)
        pl.semaphore_wait(barrier, len(peers))
        moves = exchange.moves(ins, outs, sems[:3])
        moves.start()
        later = []
        if forward:
            onward = _Moves(sems[3], sems[4], None)
            forward[1](onward, ins, outs)
            moves.wait_arrivals(first=True)
            onward.start()
            moves.wait_arrivals(first=False)
            onward.wait_arrivals()
            later.append(onward)
        else:
            moves.wait_arrivals()
        if to_sibling:
            passed = _Moves(*sems[-2:], None)
            to_sibling[1](passed, ins, outs)
            passed.start()
            passed.wait_arrivals()
            later.append(passed)
        for mv in later + [moves]:
            mv.wait_sent()

    pl.kernel(launch, mesh=plsc.ScalarSubcoreMesh(axis_name="sequencer", num_cores=1), name=name,
              scratch_types=tuple(exchange.scratch() + relay_scratch),
              compiler_params=pltpu.CompilerParams(collective_id=collective_id))()
    return [o[...] for o in outs[:len(outs) - exchange.n_staging]]


def _row_major_copy(a, name):
    r, c = a.shape
    tr = _row_tile(r)

    def body(a_ref, o_ref):
        o_ref[...] = a_ref[...]

    blk = pl.BlockSpec((tr, c), lambda i: (i, 0))
    return pl.pallas_call(body, name=name, grid=(r // tr,), in_specs=[blk], out_specs=blk,
                          out_shape=jax.ShapeDtypeStruct(a.shape, a.dtype), compiler_params=_params("parallel"))(a)


def _is_chip(x, y, chip):
    return jnp.logical_and(x == chip // 2, y == chip % 2)


def _gather_exchange(from_chip, from_all, split=()):
    n1, n2 = len(from_chip), len(from_all)
    near = CHIP_FLIPS[:2]

    def quarters(t, c, first, count=1):
        n = from_chip[t][1].shape[0] // 4
        return pl.ds((2 * c + first) * n, count * n)

    def build(mv, ins, outs):
        x, y, c = _position()
        me = 2 * x + y
        for t, (chip, _) in enumerate(from_chip):
            if t not in split:
                mv.copy(ins[t], outs[t], cond=_is_chip(x, y, chip))
        for t in range(n2):
            mv.copy(ins[n1 + t], outs[n1 + t].at[me])
        for t in split:
            for first in (True, False):
                for f, (fx, fy) in enumerate(near):
                    px, py = _flip(x, fx), _flip(y, fy)
                    part = quarters(t, c, f if first else 1 - f)
                    mv.send(ins[t].at[part], outs[t].at[part], (px, py, c), landing=outs[t].at[part], first=first,
                            send_if=_is_chip(x, y, from_chip[t][0]), recv_if=_is_chip(px, py, from_chip[t][0]))
        for fx, fy in CHIP_FLIPS:
            px, py = _flip(x, fx), _flip(y, fy)
            peer = (px, py, c)
            for t, (chip, _) in enumerate(from_chip):
                if t not in split:
                    mv.send(ins[t], outs[t], peer, landing=outs[t],
                            send_if=_is_chip(x, y, chip), recv_if=_is_chip(px, py, chip))
            for t in range(n2):
                mv.send(ins[n1 + t], outs[n1 + t].at[me], peer, landing=outs[n1 + t].at[2 * px + py])

    def forward(mv, ins, outs):
        x, y, c = _position()
        for t in split:
            chip = from_chip[t][0]
            for f, (fx, fy) in enumerate(near):
                gx, gy = near[1 - f]
                part = quarters(t, c, f)
                mv.send(outs[t].at[part], outs[t].at[part], (_flip(x, gx), _flip(y, gy), c), landing=outs[t].at[part],
                        send_if=_is_chip(_flip(x, fx), _flip(y, fy), chip), recv_if=_is_chip(1 - x, 1 - y, chip))

    def to_sibling(mv, ins, outs):
        x, y, c = _position()
        for t in split:
            came = jnp.logical_not(_is_chip(x, y, from_chip[t][0]))
            mv.send(outs[t].at[quarters(t, c, 0, 2)], outs[t].at[quarters(t, c, 0, 2)], (x, y, 1 - c),
                    landing=outs[t].at[quarters(t, 1 - c, 0, 2)], send_if=came, recv_if=came)

    arrays = [a for _, a in from_chip] + list(from_all)
    shapes = [jax.ShapeDtypeStruct(a.shape, a.dtype) for _, a in from_chip]
    shapes += [jax.ShapeDtypeStruct((N_CHIPS,) + a.shape, a.dtype) for a in from_all]
    n_remote = len(CHIP_FLIPS) * (n1 - len(split) + n2) + 2 * len(near) * len(split)
    relays = ((len(near) * len(split), forward), (len(split), to_sibling)) if split else None
    return _Exchange(arrays, shapes, n_remote, n1 + n2, build, relays, in_place=split)


def _scatter_exchange(to_chip, to_all, via_neighbours=False):
    n1, n2 = len(to_chip), len(to_all)
    near = CHIP_FLIPS[:2]
    direct = near if via_neighbours else CHIP_FLIPS

    def half(t, g):
        n = to_chip[t][1].shape[0] // 2
        return pl.ds(g * n, n)

    def build(mv, ins, outs):
        x, y, c = _position()
        if via_neighbours:
            for t, (chip, _) in enumerate(to_chip):
                for g, (gx, gy) in enumerate(near):
                    ox, oy = near[1 - g]
                    mv.send(ins[t].at[half(t, g)], outs[n1 + n2 + t], (_flip(x, gx), _flip(y, gy), c),
                            landing=outs[n1 + n2 + t], first=True, send_if=_is_chip(1 - x, 1 - y, chip),
                            recv_if=_is_chip(_flip(x, ox), _flip(y, oy), chip))
        for f, (fx, fy) in enumerate(CHIP_FLIPS):
            px, py = _flip(x, fx), _flip(y, fy)
            peer = (px, py, c)
            if (fx, fy) in direct:
                for t, (chip, _) in enumerate(to_chip):
                    mv.send(ins[t], outs[t].at[f], peer, landing=outs[t].at[f],
                            send_if=_is_chip(px, py, chip), recv_if=_is_chip(x, y, chip))
            for t in range(n2):
                mv.send(ins[n1 + t].at[2 * px + py], outs[n1 + t].at[f], peer, landing=outs[n1 + t].at[f])

    def forward(mv, ins, outs):
        x, y, c = _position()
        for t, (chip, _) in enumerate(to_chip):
            for g in range(len(near)):
                ox, oy = near[1 - g]
                far_slot = outs[t].at[len(near)].at[half(t, g)]
                mv.send(outs[n1 + n2 + t], far_slot, (_flip(x, ox), _flip(y, oy), c), landing=far_slot,
                        send_if=_is_chip(_flip(x, ox), _flip(y, oy), chip), recv_if=_is_chip(x, y, chip))

    arrays = [a for _, a in to_chip] + list(to_all)
    shapes = [jax.ShapeDtypeStruct((len(CHIP_FLIPS),) + a.shape, a.dtype) for _, a in to_chip]
    shapes += [jax.ShapeDtypeStruct((len(CHIP_FLIPS),) + a.shape[1:], a.dtype) for a in to_all]
    if not via_neighbours:
        return _Exchange(arrays, shapes, len(CHIP_FLIPS) * (n1 + n2), 0, build)
    shapes += [jax.ShapeDtypeStruct((a.shape[0] // 2, a.shape[1]), a.dtype) for _, a in to_chip]
    return _Exchange(arrays, shapes, 2 * len(near) * n1 + len(CHIP_FLIPS) * n2, 0, build,
                     relays=((len(near) * n1, forward), None), n_staging=n1)


def _swap_sibling(tensors, name):
    n = len(tensors)

    def body(*refs):
        ins, outs = refs[:n], refs[n:2 * n]
        send_sems, recv_sems = refs[2 * n:]
        x, y, c = _position()
        copies = [pltpu.make_async_remote_copy(
            src_ref=ins[t], dst_ref=outs[t], send_sem=send_sems.at[t], recv_sem=recv_sems.at[t],
            device_id=(x, y, 1 - c), device_id_type=MESH) for t in range(n)]
        for cp in copies:
            cp.start()
        for cp in copies:
            cp.wait_recv()
        for cp in copies:
            cp.wait_send()

    return pl.pallas_call(
        body, name=name, in_specs=[ANY] * n, out_specs=[ANY] * n,
        out_shape=[jax.ShapeDtypeStruct(a.shape, a.dtype) for a in tensors],
        scratch_shapes=[pltpu.SemaphoreType.DMA((n,)), pltpu.SemaphoreType.DMA((n,))],
        compiler_params=pltpu.CompilerParams(has_side_effects=True),
    )(*tensors)


def _pair_halves(g):
    r, cols = g.shape
    half = r // 2

    def body(g_ref, o_ref, mine, theirs, send_sem, recv_sem, local_sem):
        x, y, c = _position()
        away = pltpu.make_async_remote_copy(
            src_ref=g_ref.at[pl.ds((1 - c) * half, half)], dst_ref=theirs, send_sem=send_sem, recv_sem=recv_sem,
            device_id=(x, y, 1 - c), device_id_type=MESH)
        kept = pltpu.make_async_copy(g_ref.at[pl.ds(c * half, half)], mine, local_sem)
        away.start()
        kept.start()
        kept.wait()
        away.wait_recv()
        o_ref[...] = (mine[...] + theirs[...]).astype(BF16)
        away.wait_send()

    return pl.pallas_call(
        body, name="pair_halves", in_specs=[ANY], out_specs=pl.BlockSpec(memory_space=pltpu.VMEM),
        out_shape=jax.ShapeDtypeStruct((half, cols), BF16),
        scratch_shapes=[pltpu.VMEM((half, cols), F32), pltpu.VMEM((half, cols), F32),
                        pltpu.SemaphoreType.DMA(()), pltpu.SemaphoreType.DMA(()), pltpu.SemaphoreType.DMA(())],
        compiler_params=pltpu.CompilerParams(has_side_effects=True, vmem_limit_bytes=VMEM_LIMIT),
    )(g)


def _allreduce_small(slab):
    stages = 3

    def body(x_ref, o_ref, buf, send_sems, recv_sems):
        x, y, c = _position()
        peers = ((1 - x, y, c), (x, 1 - y, c), (x, y, 1 - c))
        o_ref[...] = x_ref[...]
        for k, peer in enumerate(peers):
            cp = pltpu.make_async_remote_copy(src_ref=o_ref, dst_ref=buf.at[k], send_sem=send_sems.at[k],
                                              recv_sem=recv_sems.at[k], device_id=peer, device_id_type=MESH)
            cp.start()
            cp.wait()
            o_ref[...] = o_ref[...] + buf[k]

    return pl.pallas_call(
        body, name="allreduce_small",
        in_specs=[pl.BlockSpec(memory_space=pltpu.VMEM)], out_specs=pl.BlockSpec(memory_space=pltpu.VMEM),
        out_shape=jax.ShapeDtypeStruct(slab.shape, slab.dtype),
        scratch_shapes=[pltpu.VMEM((stages,) + slab.shape, slab.dtype),
                        pltpu.SemaphoreType.DMA((stages,)), pltpu.SemaphoreType.DMA((stages,))],
        compiler_params=pltpu.CompilerParams(has_side_effects=True),
    )(slab)


def _row_tile(r):
    return min(r, 256)


def _sum4(stack, recv, me):
    _, r, c = stack.shape
    tr = _row_tile(r)

    def body(me_ref, own_ref, recv_ref, o_ref):
        o_ref[...] = (((own_ref[...] + recv_ref[0].astype(F32)) + recv_ref[1].astype(F32))
                      + recv_ref[2].astype(F32))

    return pl.pallas_call(
        body, name="sum_partials",
        grid_spec=pltpu.PrefetchScalarGridSpec(
            num_scalar_prefetch=1, grid=(r // tr,),
            in_specs=[pl.BlockSpec((None, tr, c), lambda i, me_ref: (me_ref[0], i, 0)),
                      pl.BlockSpec((len(CHIP_FLIPS), tr, c), lambda i, me_ref: (0, i, 0))],
            out_specs=pl.BlockSpec((tr, c), lambda i, me_ref: (i, 0))),
        out_shape=jax.ShapeDtypeStruct((r, c), F32), compiler_params=_params("parallel"),
    )(me, stack, recv)


def _sum_block(own, recv):
    r, c = own.shape
    tr = _row_tile(r)

    def body(own_ref, recv_ref, o_ref):
        o_ref[...] = (((own_ref[...] + recv_ref[0].astype(F32)) + recv_ref[1].astype(F32))
                      + recv_ref[2].astype(F32))

    return pl.pallas_call(
        body, name="sum_block", grid=(r // tr,),
        in_specs=[pl.BlockSpec((tr, c), lambda i: (i, 0)), pl.BlockSpec((len(CHIP_FLIPS), tr, c), lambda i: (0, i, 0))],
        out_specs=pl.BlockSpec((tr, c), lambda i: (i, 0)),
        out_shape=jax.ShapeDtypeStruct((r, c), F32), compiler_params=_params("parallel"),
    )(own, recv)


def _sum_half(own, recv, core):
    r, c = own.shape
    tr = _row_tile(r // 2)
    per_half = r // 2 // tr

    def body(core_ref, own_ref, recv_ref, o_ref):
        mine = pl.program_id(0) // per_half == core_ref[0]

        @pl.when(mine)
        def _():
            o_ref[...] = (((own_ref[...] + recv_ref[0].astype(F32)) + recv_ref[1].astype(F32))
                          + recv_ref[2].astype(F32))

        @pl.when(jnp.logical_not(mine))
        def _():
            o_ref[...] = own_ref[...]

    return pl.pallas_call(
        body, name="sum_half",
        grid_spec=pltpu.PrefetchScalarGridSpec(
            num_scalar_prefetch=1, grid=(r // tr,),
            in_specs=[pl.BlockSpec((tr, c), lambda i, core: (i, 0)),
                      pl.BlockSpec((len(CHIP_FLIPS), tr, c), lambda i, core: (0, i % per_half, 0))],
            out_specs=pl.BlockSpec((tr, c), lambda i, core: (i, 0))),
        out_shape=jax.ShapeDtypeStruct((r, c), F32), compiler_params=_params("parallel"),
    )(core, own, recv)


def _adamw_math(w, g, m, v):
    m = ADAM_B1 * m + (1.0 - ADAM_B1) * g
    v = ADAM_B2 * v + (1.0 - ADAM_B2) * (g * g)
    m_hat = m / (1.0 - ADAM_B1 ** ADAM_STEP)
    v_hat = v / (1.0 - ADAM_B2 ** ADAM_STEP)
    delta = -ADAM_LR * (m_hat / (jnp.sqrt(v_hat) + ADAM_EPS) + ADAM_WD * w)
    return delta, m, v


def _adamw(w, m, v, g_parts, name):
    r, c = w.shape
    tr = _row_tile(r)
    n = len(g_parts)

    def body(*refs):
        w_ref, m_ref, v_ref = refs[:3]
        g_refs = refs[3:3 + n]
        g_out, d_out, m_out, v_out, zero_out = refs[3 + n:]
        g = g_refs[0][...]
        for ref in g_refs[1:]:
            g = g + ref[...]
        g_out[...] = g
        d_out[...], m_out[...], v_out[...] = _adamw_math(w_ref[...], g, m_ref[...], v_ref[...])
        zero_out[0] = 0

    blk = pl.BlockSpec((tr, c), lambda i: (i, 0))
    return pl.pallas_call(
        body, name=name, grid=(r // tr,), in_specs=[blk] * (3 + n),
        out_specs=[blk] * 4 + [pl.BlockSpec(memory_space=pltpu.SMEM)],
        out_shape=[jax.ShapeDtypeStruct((r, c), F32)] * 4 + [jax.ShapeDtypeStruct((1,), jnp.int32)],
        compiler_params=_params("arbitrary"),
    )(w, m, v, *g_parts)


def _adamw_small(total, w, m, v):
    sizes = [w[n].size for n in SMALL]
    flat = lambda d: [d[n].reshape(1, -1) for n in SMALL]
    k = len(SMALL)

    def body(*refs):
        total_ref, w_refs, m_refs, v_refs = refs[0], refs[1:1 + k], refs[1 + k:1 + 2 * k], refs[1 + 2 * k:1 + 3 * k]
        outs = refs[1 + 3 * k:]
        for i, size in enumerate(sizes):
            g = total_ref[i:i + 1, 0:size]
            outs[i][...] = g
            outs[k + i][...], outs[2 * k + i][...], outs[3 * k + i][...] = _adamw_math(
                w_refs[i][...], g, m_refs[i][...], v_refs[i][...])

    res = pl.pallas_call(
        body, name="adamw_small", out_shape=[jax.ShapeDtypeStruct((1, size), F32) for size in sizes] * 4,
        compiler_params=_params(),
    )(total, *flat(w), *flat(m), *flat(v))
    return [{n: res[j * k + i].reshape(w[n].shape) for i, n in enumerate(SMALL)} for j in range(4)]


SHARDED = ("w_in", "w_lora_up", "a_lora_up", "w_out_a", "w_out_b", "w_out")
ROW_SHARDED = ("w_out",)
SMALL = ("norm_g", "shift_mu", "w0", "a0", "k_k", "k_a", "r_k", "lnx_w", "lnx_b", "f_bias", "q_norm_g", "k_norm_g",
         "final_norm_g")
WEIGHTS = ("norm_g", "w_in", "shift_mu", "w_lora_up", "w0", "a_lora_up", "a0", "k_k", "k_a", "r_k", "lnx_w", "lnx_b",
           "f_bias", "q_norm_g", "k_norm_g", "w_out_a", "w_out_b", "w_out", "final_norm_g")
SLAB_ROWS = 16
SLAB_COLS = SEC


def _to_slab(named, extra=None):
    rows = [jnp.pad(named[n].reshape(1, -1), ((0, 0), (0, SLAB_COLS - named[n].size))) for n in SMALL]
    if extra is not None:
        rows.append(jnp.pad(extra.reshape(1, -1), ((0, 0), (0, SLAB_COLS - extra.size))))
    rows.append(jnp.zeros((SLAB_ROWS - len(rows), SLAB_COLS), F32))
    return jnp.concatenate(rows, axis=0)


def _by_chip(g, name):
    if name in ROW_SHARDED:
        return g.reshape(N_CHIPS, g.shape[0] // N_CHIPS, g.shape[1])
    r, c = g.shape
    return g.reshape(r, N_CHIPS, c // N_CHIPS).transpose(1, 0, 2)


def _from_chips(stack, name):
    if name in ROW_SHARDED:
        return stack.reshape(-1, stack.shape[2])
    _, r, c = stack.shape
    return stack.transpose(1, 0, 2).reshape(r, N_CHIPS * c)


def kernel(x, norm_g, w_in, shift_mu, w_lora_up, w0, a_lora_up, a0, k_k, k_a, r_k, lnx_w, lnx_b, f_bias, q_norm_g, k_norm_g, w_out_a, w_out_b, w_out, final_norm_g, loss_target, m_norm_g, m_w_in, m_shift_mu, m_w_lora_up, m_w0, m_a_lora_up, m_a0, m_k_k, m_k_a, m_r_k, m_lnx_w, m_lnx_b, m_f_bias, m_q_norm_g, m_k_norm_g, m_w_out_a, m_w_out_b, m_w_out, m_final_norm_g, v_norm_g, v_w_in, v_shift_mu, v_w_lora_up, v_w0, v_a_lora_up, v_a0, v_k_k, v_k_a, v_r_k, v_lnx_w, v_lnx_b, v_f_bias, v_q_norm_g, v_k_norm_g, v_w_out_a, v_w_out_b, v_w_out, v_final_norm_g):
    w = dict(norm_g=norm_g, w_in=w_in, shift_mu=shift_mu, w_lora_up=w_lora_up, w0=w0, a_lora_up=a_lora_up, a0=a0,
             k_k=k_k, k_a=k_a, r_k=r_k, lnx_w=lnx_w, lnx_b=lnx_b, f_bias=f_bias, q_norm_g=q_norm_g,
             k_norm_g=k_norm_g, w_out_a=w_out_a, w_out_b=w_out_b, w_out=w_out, final_norm_g=final_norm_g)
    m = dict(norm_g=m_norm_g, w_in=m_w_in, shift_mu=m_shift_mu, w_lora_up=m_w_lora_up, w0=m_w0,
             a_lora_up=m_a_lora_up, a0=m_a0, k_k=m_k_k, k_a=m_k_a, r_k=m_r_k, lnx_w=m_lnx_w, lnx_b=m_lnx_b,
             f_bias=m_f_bias, q_norm_g=m_q_norm_g, k_norm_g=m_k_norm_g, w_out_a=m_w_out_a, w_out_b=m_w_out_b,
             w_out=m_w_out, final_norm_g=m_final_norm_g)
    v = dict(norm_g=v_norm_g, w_in=v_w_in, shift_mu=v_shift_mu, w_lora_up=v_w_lora_up, w0=v_w0,
             a_lora_up=v_a_lora_up, a0=v_a0, k_k=v_k_k, k_a=v_k_a, r_k=v_r_k, lnx_w=v_lnx_w, lnx_b=v_lnx_b,
             f_bias=v_f_bias, q_norm_g=v_q_norm_g, k_norm_g=v_k_norm_g, w_out_a=v_w_out_a, w_out_b=v_w_out_b,
             w_out=v_w_out, final_norm_g=v_final_norm_g)
    shapes = {n: w[n].shape for n in WEIGHTS}

    shard = {n: w[n][0].astype(BF16) for n in SHARDED}
    late = ("w_out_a", "w_out_b", "w_out")
    loras = ("w_lora_up", "a_lora_up")
    w_in_head, w_in_tail = shard["w_in"][:, :A_TAIL], shard["w_in"][:, A_TAIL:]
    shard0, shard1_head, up_stack, aup_stack = _run_on_sequencer(_gather_exchange(
        [(0, shard["w_in"]), (1, w_in_head)], [shard[n] for n in loras], split=(0, 1)), "gather_early", 2)
    moments = (_row_major_copy(m["w_in"][0], "m_w_in_rows"), _row_major_copy(v["w_in"][0], "v_w_in_rows"))
    shard0, moments = lax.optimization_barrier((shard0, moments))
    w_a = jnp.concatenate([shard0, shard1_head], axis=1)

    def late_weights(arrived):
        shard1_tail, shard2, shard3 = arrived[:3]
        w_b = jnp.concatenate([shard1_tail, shard2[:, :B_TAIL], jnp.zeros((D_MODEL, SEC - FOX_REAL), BF16)], axis=1)
        w_g = jnp.concatenate([shard2[:, B_TAIL:], shard3], axis=1)
        return (w_b, w_g, *[_from_chips(s, n) for n, s in zip(late, arrived[3:])])

    own = {}
    cut = {"block0": lambda: own["dw_a"][:, :SHARD_COLS], "head1": lambda: own["dw_a"][:, SHARD_COLS:],
           "tail1": lambda: own["dw_b"][:, :B_HEAD],
           "block2": lambda: jnp.concatenate([own["dw_b"][:, B_HEAD:FOX_REAL], own["dw_g"][:, :G_HEAD]], axis=1),
           "block3": lambda: own["dw_g"][:, G_HEAD:]}

    def bwd_exchange(dw_b, dw_g, dwa, dwb, dwo):
        own.update(dw_b=dw_b, dw_g=dw_g)
        own.update({n: _by_chip(g, n) for n, g in zip(late, (dwa, dwb, dwo))})
        return _scatter_exchange([(1, cut["tail1"]().astype(BF16)), (2, cut["block2"]().astype(BF16)),
                                  (3, cut["block3"]().astype(BF16))], [own[n].astype(BF16) for n in late])

    def tail_exchange(dw_a, dw_up, da_up):
        own.update(dw_a=dw_a)
        own.update({n: _by_chip(g, n) for n, g in zip(loras, (dw_up, da_up))})
        pair = _pair_halves(dw_a)
        return _scatter_exchange([(0, pair[:, :SHARD_COLS]), (1, pair[:, SHARD_COLS:])],
                                 [own[n].astype(BF16) for n in loras], via_neighbours=True)

    small = {n: w[n] for n in SMALL}
    loss_vec, grad_x, grads, sent, sent_last = _device_grads(
        x[0], loss_target[0], small, w_a, _from_chips(up_stack, "w_lora_up"), _from_chips(aup_stack, "a_lora_up"),
        late_weights, _gather_exchange([(1, w_in_tail), (2, shard["w_in"]), (3, shard["w_in"])], [shard[n] for n in late]),
        bwd_exchange, tail_exchange)

    total = _allreduce_small(_to_slab(grads, extra=loss_vec))
    loss = (0.5 / D_MODEL) * jnp.sum(total[len(SMALL)])
    out_g, out_d, out_m, out_v = _adamw_small(total, w, m, v)

    xpos, ypos, cpos = _position()
    me = (2 * xpos + ypos).astype(jnp.int32).reshape(1)
    core = cpos.astype(jnp.int32).reshape(1)
    core_sum, theirs = {}, {}

    def update(n):
        m_n, v_n = moments if n == "w_in" else (m[n][0], v[n][0])
        g, d, m2, v2, zero = _adamw(w[n][0], m_n, v_n, [core_sum[n], theirs[n]], "adamw_" + n)
        out_g[n], out_d[n], out_m[n], out_v[n] = (a.reshape(shapes[n]) for a in (g, d, m2, v2))
        return zero

    sent_last, out_d["norm_g"] = lax.optimization_barrier((sent_last, out_d["norm_g"]))
    core_sum["w_in"] = lax.switch(me[0], [
        lambda: _sum_half(cut["block0"](), sent_last[0], core),
        lambda: jnp.concatenate([_sum_half(cut["head1"](), sent_last[1], core), _sum_block(cut["tail1"](), sent[0])],
                                axis=1),
        lambda: _sum_block(cut["block2"](), sent[1]),
        lambda: _sum_block(cut["block3"](), sent[2])])
    core_sum.update({n: _sum4(own[n], r, me) for n, r in zip(loras, sent_last[2:])})
    rest = ("w_in",) + loras
    theirs.update(zip(rest, _swap_sibling([core_sum[n] for n in rest], "swap_sibling")))
    after_w_in = me + [update(n) for n in rest][0]
    core_sum.update({n: _sum4(own[n], r, after_w_in) for n, r in zip(late, sent[3:])})
    theirs.update(zip(late, _swap_sibling([core_sum[n] for n in late], "swap_sibling_late")))
    for n in late:
        update(n)

    return (loss, grad_x.reshape(x.shape), *[out_g[n] for n in WEIGHTS], *[out_d[n] for n in WEIGHTS],
            *[out_m[n] for n in WEIGHTS], *[out_v[n] for n in WEIGHTS])
```

```python
import functools
import math

import jax
import jax.numpy as jnp
from jax import lax
from jax.experimental import pallas as pl
from jax.experimental.pallas import tpu as pltpu
from jax.experimental.pallas import tpu_sc as plsc

F32 = jnp.float32
BF16 = jnp.bfloat16

D_MODEL = 1024
D_HALF = 512
HEAD = 64
N_HEADS = 8
LORA = 64
RWKV_COLS = 2176
FOX_REAL = 2056
SEC = 2176
GATE_COLS = 2048
IN_COLS = 6280
N_CHIPS = 4
SHARD_COLS = IN_COLS // N_CHIPS
A_TAIL = RWKV_COLS - SHARD_COLS
B_HEAD = SHARD_COLS - A_TAIL
B_TAIL = FOX_REAL - B_HEAD
G_HEAD = SHARD_COLS - B_TAIL
RMS_EPS = 1e-6
LNX_EPS = 64e-5
ATT_SCALE = HEAD ** -0.5
NEG = -1e30

ADAM_LR = 0.001
ADAM_B1 = 0.9
ADAM_B2 = 0.999
ADAM_EPS = 1e-08
ADAM_WD = 0.01
ADAM_STEP = 10

LANES = 128
SUBLANES = 8
VMEM_LIMIT = 56 * 1024 * 1024
MESH = pl.DeviceIdType.MESH


def _params(*sem):
    return pltpu.CompilerParams(dimension_semantics=sem if sem else None, vmem_limit_bytes=VMEM_LIMIT)


def _sigmoid(x):
    return 1.0 / (1.0 + jnp.exp(-x))


def _log_sigmoid(x):
    return jnp.minimum(x, 0.0) - jnp.log(1.0 + jnp.exp(-jnp.abs(x)))


def _head_ones():
    r = lax.broadcasted_iota(jnp.int32, (LANES, LANES), 0) >> 6
    c = lax.broadcasted_iota(jnp.int32, (LANES, LANES), 1) >> 6
    return (r == c).astype(BF16)


def _split3(x):
    hi = x.astype(BF16)
    r1 = x - hi.astype(F32)
    mid = r1.astype(BF16)
    lo = (r1 - mid.astype(F32)).astype(BF16)
    return hi, mid, lo


def _exact_dot(x, ones_bf16, ones_first=False):
    out = None
    for piece in _split3(x):
        if ones_first:
            t = jnp.dot(ones_bf16, piece, preferred_element_type=F32)
        else:
            t = jnp.dot(piece, ones_bf16, preferred_element_type=F32)
        out = t if out is None else out + t
    return out


def _head_sum(x, bd):
    n = x.shape[1] // LANES
    parts = [_exact_dot(x[:, i * LANES:(i + 1) * LANES], bd) for i in range(n)]
    return parts[0] if n == 1 else jnp.concatenate(parts, axis=1)


def _dot_nt(a, b):
    return lax.dot_general(a, b, (((1,), (1,)), ((), ())), preferred_element_type=F32)


def _dot_tn(a, b):
    return lax.dot_general(a, b, (((0,), (0,)), ((), ())), preferred_element_type=F32)


def _colsum(x):
    return jnp.sum(x, axis=0, keepdims=True)


def _matmul_tn_acc(at, b, name, tk=512):
    m, k = at.shape
    n = b.shape[1]

    def body(a_ref, b_ref, o_ref):
        j = pl.program_id(0)

        @pl.when(j == 0)
        def _():
            o_ref[...] = jnp.zeros_like(o_ref)

        o_ref[...] += jnp.dot(a_ref[...], b_ref[...].astype(BF16), preferred_element_type=F32)

    return pl.pallas_call(
        body, name=name, grid=(k // tk,),
        in_specs=[pl.BlockSpec((m, tk), lambda j: (0, j)), pl.BlockSpec((tk, n), lambda j: (j, 0))],
        out_specs=pl.BlockSpec((m, n), lambda j: (0, 0)),
        out_shape=jax.ShapeDtypeStruct((m, n), F32), compiler_params=_params("arbitrary"),
    )(at, b)


def _inproj_bwd(du_a, du_b, du_g, w_a, w_b, w_g, x, dx2, g, exchange=None, tm=256):
    s, d = x.shape
    nb = s // tm

    def body(*refs):
        ((da_ref, db_ref, dg_ref, wa_ref, wb_ref, wg_ref, x_ref, dx2_ref, g_ref), (gx_ref, gg_ref), _,
         moves) = _split_refs(refs, 9, 2, exchange)
        i = pl.program_id(0)
        if moves:
            moves.start(also=(i == 0))

        @pl.when(i == 0)
        def _():
            gg_ref[...] = jnp.zeros_like(gg_ref)

        dh = _dot_nt(da_ref[...].astype(BF16), wa_ref[...])
        dh += _dot_nt(db_ref[...].astype(BF16), wb_ref[...])
        dh += _dot_nt(dg_ref[...].astype(BF16), wg_ref[...])
        xv = x_ref[...]
        r = lax.rsqrt(jnp.mean(xv * xv, axis=-1, keepdims=True) + RMS_EPS)
        xh = xv * r
        gg_ref[...] += _colsum(dh * xh)
        dxh = dh * g_ref[...]
        gx_ref[...] = dx2_ref[...] + r * (dxh - xh * jnp.mean(dxh * xh, axis=-1, keepdims=True))
        if moves:
            moves.wait(also=(i == nb - 1))

    row = lambda w: pl.BlockSpec((tm, w), lambda i: (i, 0))
    full = lambda a: pl.BlockSpec(a.shape, lambda i: (0, 0))
    ex_in = exchange.operands if exchange else []
    ex_out = exchange.out_shapes if exchange else []
    res = pl.pallas_call(
        body, name="inproj_bwd", grid=(nb,),
        in_specs=[row(SEC), row(SEC), row(GATE_COLS), full(w_a), full(w_b), full(w_g), row(d), row(d), full(g)]
                 + [ANY] * len(ex_in),
        out_specs=[row(d), pl.BlockSpec((1, d), lambda i: (0, 0))] + [ANY] * len(ex_out),
        out_shape=[jax.ShapeDtypeStruct((s, d), F32), jax.ShapeDtypeStruct((1, d), F32)] + ex_out,
        scratch_shapes=exchange.scratch() if exchange else [],
        compiler_params=_params("arbitrary"),
    )(du_a, du_b, du_g, w_a, w_b, w_g, x, dx2, g, *ex_in)
    return res[0], res[1], list(res[2:])


def _rwkv_elementwise(ua, prev_row, first, mu, wl, w0, a0, kkw, kaw, bd):
    tm = ua.shape[0]
    rows = lax.broadcasted_iota(jnp.int32, (tm, 1), 0)
    prev = jnp.where(first, jnp.zeros_like(prev_row), prev_row)
    shifted = jnp.where(rows == 0, prev, pltpu.roll(ua, 1, 0))
    delta = shifted - ua
    us = ua + delta * mu
    r = us[:, 0:512]
    k0 = us[:, 512:1024]
    v = us[:, 1024:1536]
    lo = us[:, 1536:1664]
    gate = us[:, 1664:2176]
    lane = lax.broadcasted_iota(jnp.int32, (1, LANES), 1)
    th = jnp.tanh(lo)
    lin = jnp.where(lane < LORA, th, lo)
    ll = jnp.dot(lin.astype(BF16), wl, preferred_element_type=F32)
    sz = _sigmoid(w0 + ll[:, :512])
    e = sz * math.exp(-0.5)
    dec = jnp.exp(-e)
    a = _sigmoid(a0 + ll[:, 512:])
    kk0 = k0 * kkw
    ss = _head_sum(kk0 * kk0, bd)
    nrm = jnp.maximum(jnp.sqrt(ss), 1e-12)
    kk = kk0 / nrm
    k = k0 * (1.0 + (a - 1.0) * kaw)
    return dict(delta=delta, us=us, r=r, k0=k0, v=v, lo=lo, gate=gate, th=th, lin=lin, sz=sz, e=e, dec=dec,
                a=a, kk0=kk0, ss=ss, nrm=nrm, kk=kk, k=k)


def _rwkv_front(x, g, w_a, mu, wl, w0, a0, kkw, kaw, tm=256):
    s, d = x.shape

    def body(x_ref, g_ref, wa_ref, mu_ref, wl_ref, w0_ref, a0_ref, kkw_ref, kaw_ref,
             h_ref, ua_ref, r_ref, w_ref, k_ref, v_ref, a_ref, b_ref, gate_ref, last_row):
        i = pl.program_id(0)
        xv = x_ref[...]
        h = (xv * lax.rsqrt(jnp.mean(xv * xv, axis=-1, keepdims=True) + RMS_EPS) * g_ref[...]).astype(BF16)
        h_ref[...] = h
        ua = jnp.dot(h, wa_ref[...], preferred_element_type=F32)
        ua_ref[...] = ua

        @pl.when(i == 0)
        def _():
            last_row[...] = jnp.zeros_like(last_row)

        f = _rwkv_elementwise(ua, last_row[...], i == 0, mu_ref[...], wl_ref[...], w0_ref[...],
                              a0_ref[...], kkw_ref[...], kaw_ref[...], _head_ones())
        last_row[...] = ua[tm - 1:tm, :]
        r_ref[...] = f["r"]
        w_ref[...] = f["dec"]
        k_ref[...] = f["k"]
        v_ref[...] = f["v"]
        a_ref[...] = -f["kk"]
        b_ref[...] = f["kk"] * f["a"]
        gate_ref[...] = f["gate"]

    vec = lambda w: pl.BlockSpec((1, w), lambda i: (0, 0))
    row = lambda w: pl.BlockSpec((tm, w), lambda i: (i, 0))
    return pl.pallas_call(
        body, name="rwkv_front", grid=(s // tm,),
        in_specs=[row(d), vec(d), pl.BlockSpec(w_a.shape, lambda i: (0, 0), pipeline_mode=pl.Buffered(1)),
                  vec(SEC), pl.BlockSpec((LANES, 2 * D_HALF), lambda i: (0, 0)),
                  vec(D_HALF), vec(D_HALF), vec(D_HALF), vec(D_HALF)],
        out_specs=[row(d), row(SEC)] + [row(D_HALF)] * 7,
        out_shape=[jax.ShapeDtypeStruct((s, d), BF16), jax.ShapeDtypeStruct((s, SEC), F32)]
                  + [jax.ShapeDtypeStruct((s, D_HALF), F32)] * 7,
        scratch_shapes=[pltpu.VMEM((1, SEC), F32)],
        compiler_params=_params("arbitrary"),
    )(x, g, w_a, mu, wl, w0, a0, kkw, kaw)


SCAN_TB = 128
N_PAIRS = 4


def _pair_sum(x, left):
    s_l = jnp.sum(jnp.where(left, x, 0.0), axis=1, keepdims=True)
    s_r = jnp.sum(jnp.where(left, 0.0, x), axis=1, keepdims=True)
    return jnp.where(left, s_l, s_r)


def _pair_dot(x, row_l, row_r, left):
    s_l = jnp.sum(x * row_l, axis=1, keepdims=True)
    s_r = jnp.sum(x * row_r, axis=1, keepdims=True)
    return jnp.where(left, s_l, s_r)


def _halves(rows8):
    lane = lax.broadcasted_iota(jnp.int32, rows8.shape, 1)
    keep_left = (lane & (LANES - 1)) < HEAD
    return jnp.where(keep_left, rows8, 0.0), jnp.where(keep_left, 0.0, rows8)


def _quad_consts():
    lane = lax.broadcasted_iota(jnp.int32, (HEAD, 2 * LANES), 1)
    rowi = lax.broadcasted_iota(jnp.int32, (HEAD, 2 * LANES), 0)
    diag2 = rowi == (lane & (HEAD - 1))
    r = lax.broadcasted_iota(jnp.int32, (2 * LANES, 2 * LANES), 0) >> 6
    c = lax.broadcasted_iota(jnp.int32, (2 * LANES, 2 * LANES), 1) >> 6
    return diag2, (r == c).astype(BF16)


def _rows_to_columns(x8, diag2, bd2):
    lhs = jnp.concatenate([jnp.where(diag2, x8[i:i + 1], 0.0).astype(BF16) for i in range(SUBLANES)], axis=0)
    return jnp.dot(lhs, bd2, preferred_element_type=F32)


def _diag_rows(qtile, diag2, bd2, sub_row2):
    res = jnp.dot(qtile, bd2, preferred_element_type=F32)
    out = jnp.zeros((SUBLANES, 2 * LANES), F32)
    for i in range(SUBLANES):
        out = jnp.where(sub_row2 == i, _colsum(jnp.where(diag2, res[i * HEAD:(i + 1) * HEAD], 0.0)), out)
    return out


def _store_tile(qbuf, slot, p, i, x):
    qbuf[slot, p // 2, i * HEAD:(i + 1) * HEAD, (p % 2) * LANES:(p % 2 + 1) * LANES] = x.astype(BF16)


def _left_half():
    return lax.broadcasted_iota(jnp.int32, (HEAD, LANES), 1) < HEAD


def _split_refs(refs, n_rows, n_out, exchange):
    n_in = len(exchange.operands) if exchange else 0
    n_ex_out = len(exchange.out_shapes) if exchange else 0
    refs = list(refs)
    rows, refs = refs[:n_rows], refs[n_rows:]
    ex_in, refs = refs[:n_in], refs[n_in:]
    outs, refs = refs[:n_out], refs[n_out:]
    ex_out, refs = refs[:n_ex_out], refs[n_ex_out:]
    scratch, sems = (refs[:-3], refs[-3:]) if exchange else (refs, None)
    moves = exchange.moves(ex_in, ex_out, sems) if exchange else None
    return rows, outs, scratch, moves


def _wkv_fwd(r, w, k, a, b, v, exchange=None):
    s = r.shape[0]
    tb = SCAN_TB
    nb = s // tb

    def body(*refs):
        (r_ref, w_ref, k_ref, a_ref, b_ref, v_ref), (y_ref, st_ref), (state, vbuf, qbuf), moves = _split_refs(
            refs, 6, 2, exchange)
        g = pl.program_id(0)
        if moves:
            moves.start(also=(g == 0))

        @pl.when(g == 0)
        def _():
            state[...] = jnp.zeros_like(state)
            qbuf[...] = jnp.zeros_like(qbuf)

        left = _left_half()
        diag2, bd2 = _quad_consts()
        sub_row2 = lax.broadcasted_iota(jnp.int32, (SUBLANES, 2 * LANES), 0)
        groups = tb // SUBLANES
        quads = [slice(g2 * 2 * LANES, (g2 + 1) * 2 * LANES) for g2 in range(2)]

        def rows_of(q):
            return pl.ds(pl.multiple_of(q * SUBLANES, SUBLANES), SUBLANES)

        def v_tiles(q, slot):
            v8 = v_ref[rows_of(q), :]
            for g2 in range(2):
                vbuf[slot, g2] = _rows_to_columns(v8[:, quads[g2]], diag2, bd2)

        def chain(q, slot):
            rows8 = rows_of(q)
            a8, w8, b8, k8, r8 = (x[rows8, :] for x in (a_ref, w_ref, b_ref, k_ref, r_ref))
            pairs = [slice(p * LANES, (p + 1) * LANES) for p in range(N_PAIRS)]
            a_next = pltpu.roll(a8, SUBLANES - 1, 0)
            (a8_l, a8_r), (wa8_l, wa8_r) = _halves(a8), _halves(w8 * a_next)
            ba8 =jnp.concatenate([_pair_sum(b8[:, pr] * a_next[:, pr], left[0:SUBLANES]) for pr in pairs], axis=1)
            ka8 = jnp.concatenate([_pair_sum(k8[:, pr] * a_next[:, pr], left[0:SUBLANES]) for pr in pairs], axis=1)
            sp = [state[p] for p in range(N_PAIRS)]
            for i in range(0, SUBLANES, 2):
                r0, r1 = slice(i, i + 1), slice(i + 1, i + 2)
                sums = [(_pair_dot(sp[p], a8_l[r0, pairs[p]], a8_r[r0, pairs[p]], left),
                         _pair_dot(sp[p], wa8_l[r0, pairs[p]], wa8_r[r0, pairs[p]], left)) for p in range(N_PAIRS)]
                sa0, sa1 = [s[0] for s in sums], [s[1] for s in sums]
                for p in range(N_PAIRS):
                    pr = pairs[p]
                    inner = slice((p % 2) * LANES, (p % 2 + 1) * LANES)
                    vt0 = vbuf[slot, p // 2, i * HEAD:(i + 1) * HEAD, inner]
                    vt1 = vbuf[slot, p // 2, (i + 1) * HEAD:(i + 2) * HEAD, inner]
                    sa_next = sa1[p] + sa0[p] * ba8[r0, pr] + vt0 * ka8[r0, pr]
                    s1 = sp[p] * w8[r0, pr] + sa0[p] * b8[r0, pr] + vt0 * k8[r0, pr]
                    st_ref[q * SUBLANES + i, p] = s1
                    _store_tile(qbuf, slot, p, i, s1 * r8[r0, pr])
                    s2 = s1 * w8[r1, pr] + sa_next * b8[r1, pr] + vt1 * k8[r1, pr]
                    st_ref[q * SUBLANES + i + 1, p] = s2
                    _store_tile(qbuf, slot, p, i + 1, s2 * r8[r1, pr])
                    sp[p] = s2
            for p in range(N_PAIRS):
                state[p] = sp[p]

        def y_rows(q, slot):
            for g2 in range(2):
                y_ref[rows_of(q), quads[g2]] = _diag_rows(qbuf[slot, g2], diag2, bd2, sub_row2)

        v_tiles(0, 0)

        def two_groups(j, carry):
            q0 = 2 * j
            v_tiles(q0 + 1, 1)
            chain(q0, 0)
            y_rows(jnp.maximum(q0 - 1, 0), 1)
            v_tiles(jnp.minimum(q0 + 2, groups - 1), 0)
            chain(q0 + 1, 1)
            y_rows(q0, 0)
            return carry

        lax.fori_loop(0, groups // 2, two_groups, 0)
        y_rows(groups - 1, 1)
        if moves:
            moves.wait(also=(g == nb - 1))

    rows = pl.BlockSpec((tb, D_HALF), lambda g: (g, 0))
    ex_in = exchange.operands if exchange else []
    ex_out = exchange.out_shapes if exchange else []
    res = pl.pallas_call(
        body, name="wkv_fwd", grid=(nb,),
        in_specs=[rows] * 6 + [ANY] * len(ex_in),
        out_specs=[rows, pl.BlockSpec((tb, N_PAIRS, HEAD, LANES), lambda g: (g, 0, 0, 0))] + [ANY] * len(ex_out),
        out_shape=[jax.ShapeDtypeStruct((s, D_HALF), F32),
                   jax.ShapeDtypeStruct((s, N_PAIRS, HEAD, LANES), F32)] + ex_out,
        scratch_shapes=[pltpu.VMEM((N_PAIRS, HEAD, LANES), F32),
                        pltpu.VMEM((2, 2, SUBLANES * HEAD, 2 * LANES), F32),
                        pltpu.VMEM((2, 2, SUBLANES * HEAD, 2 * LANES), BF16)]
                       + (exchange.scratch() if exchange else []),
        compiler_params=_params("arbitrary"),
    )(r, w, k, a, b, v, *ex_in)
    return res[0], res[1], list(res[2:])


def _wkv_bwd(r, w, k, a, b, v, dy, st, exchange=None):
    s = r.shape[0]
    tb = SCAN_TB
    nb = s // tb

    def body(*refs):
        ((r_ref, w_ref, k_ref, a_ref, b_ref, v_ref, dy_ref, st_ref, before_ref),
         (dr_ref, dw_ref, dk_ref, dv_ref, da_ref, db_ref), (dstate, vbuf, qbuf, sbuf),
         moves) = _split_refs(refs, 9, 6, exchange)
        g = pl.program_id(0)
        first_block = g == nb - 1
        if moves:
            moves.start(also=(g == 0))

        @pl.when(g == 0)
        def _():
            dstate[...] = jnp.zeros_like(dstate)
            qbuf[...] = jnp.zeros_like(qbuf)

        left = _left_half()
        diag2, bd2 = _quad_consts()
        sub_row = lax.broadcasted_iota(jnp.int32, (SUBLANES, LANES), 0)
        sub_row2 = lax.broadcasted_iota(jnp.int32, (SUBLANES, 2 * LANES), 0)
        groups = tb // SUBLANES
        quads = [slice(g2 * 2 * LANES, (g2 + 1) * 2 * LANES) for g2 in range(2)]
        row_refs = (dr_ref, dw_ref, dk_ref, da_ref, db_ref)

        def rows_of(q):
            return pl.ds(pl.multiple_of(q * SUBLANES, SUBLANES), SUBLANES)

        def state_before(q, i, p):
            if i > 0:
                return st_ref[q * SUBLANES + i - 1, p]
            return jnp.where(q == 0, jnp.where(first_block, 0.0, before_ref[0, p]),
                             st_ref[jnp.maximum(q * SUBLANES - 1, 0), p])

        def column_tiles(q, slot):
            rows8 = rows_of(q)
            for kind, ref in enumerate((v_ref, dy_ref)):
                x8 = ref[rows8, :]
                for g2 in range(2):
                    vbuf[slot, kind, g2] = _rows_to_columns(x8[:, quads[g2]], diag2, bd2)
            a8 = a_ref[rows8, :]
            for i in range(SUBLANES):
                for p in range(N_PAIRS):
                    _store_tile(sbuf, 0, p, i, state_before(q, i, p) * a8[i:i + 1, p * LANES:(p + 1) * LANES])
            for g2 in range(2):
                vbuf[slot, 2, g2] = jnp.dot(sbuf[0, g2], bd2, preferred_element_type=F32)

        def chain(q, slot):
            rows8 = rows_of(q)
            a8, w8, b8, k8, r8 = (x[rows8, :] for x in (a_ref, w_ref, b_ref, k_ref, r_ref))
            b8_l, b8_r = _halves(b8)
            dsp = [dstate[p] for p in range(N_PAIRS)]
            outs = [[jnp.zeros((SUBLANES, LANES), F32) for _ in row_refs] for _ in range(N_PAIRS)]
            after = [st_ref[q * SUBLANES + SUBLANES - 1, p] for p in range(N_PAIRS)]
            for i in reversed(range(SUBLANES)):
                row = slice(i, i + 1)
                pl_ = [slice(p * LANES, (p + 1) * LANES) for p in range(N_PAIRS)]
                tile = [(p // 2, slice(i * HEAD, (i + 1) * HEAD), slice((p % 2) * LANES, (p % 2 + 1) * LANES))
                        for p in range(N_PAIRS)]
                sp = [state_before(q, i, p) for p in range(N_PAIRS)]
                dyt = [vbuf[(slot, 1) + tile[p]] for p in range(N_PAIRS)]
                ds = [dsp[p] + dyt[p] * r8[row, pl_[p]] for p in range(N_PAIRS)]
                dsa = [_pair_dot(ds[p], b8_l[row, pl_[p]], b8_r[row, pl_[p]], left) for p in range(N_PAIRS)]
                sa = [vbuf[(slot, 2) + tile[p]] for p in range(N_PAIRS)]
                for p in range(N_PAIRS):
                    ar, wr, br, kr = (x[row, pl_[p]] for x in (a8, w8, b8, k8))
                    vt = vbuf[(slot, 0) + tile[p]]
                    dsp[p] = ds[p] * wr + dsa[p] * ar
                    new = (_colsum(after[p] * dyt[p]), _colsum(ds[p] * sp[p]), _colsum(ds[p] * vt),
                           _colsum(sp[p] * dsa[p]), _colsum(ds[p] * sa[p]))
                    outs[p] = [jnp.where(sub_row == i, n, o) for n, o in zip(new, outs[p])]
                    _store_tile(qbuf, slot, p, i, ds[p] * kr)
                after = sp
            for p in range(N_PAIRS):
                dstate[p] = dsp[p]
                for ref, o in zip(row_refs, outs[p]):
                    ref[rows8, p * LANES:(p + 1) * LANES] = o

        def dv_rows(q, slot):
            for g2 in range(2):
                dv_ref[rows_of(q), quads[g2]] = _diag_rows(qbuf[slot, g2], diag2, bd2, sub_row2)

        column_tiles(groups - 1, 0)

        def two_groups(j, carry):
            q0 = groups - 1 - 2 * j
            column_tiles(q0 - 1, 1)
            chain(q0, 0)
            dv_rows(jnp.minimum(q0 + 1, groups - 1), 1)
            column_tiles(jnp.maximum(q0 - 2, 0), 0)
            chain(q0 - 1, 1)
            dv_rows(q0, 0)
            return carry

        lax.fori_loop(0, groups // 2, two_groups, 0)
        dv_rows(0, 1)
        if moves:
            moves.wait(also=(g == nb - 1))

    rows = pl.BlockSpec((tb, D_HALF), lambda g: (nb - 1 - g, 0))
    ex_in = exchange.operands if exchange else []
    ex_out = exchange.out_shapes if exchange else []
    res = pl.pallas_call(
        body, name="wkv_bwd", grid=(nb,),
        in_specs=[rows] * 7 + [pl.BlockSpec((tb, N_PAIRS, HEAD, LANES), lambda g: (nb - 1 - g, 0, 0, 0)),
                               pl.BlockSpec((1, N_PAIRS, HEAD, LANES),
                                            lambda g: (jnp.maximum((nb - 1 - g) * tb - 1, 0), 0, 0, 0))]
                 + [ANY] * len(ex_in),
        out_specs=[rows] * 6 + [ANY] * len(ex_out),
        out_shape=[jax.ShapeDtypeStruct((s, D_HALF), F32)] * 6 + ex_out,
        scratch_shapes=[pltpu.VMEM((N_PAIRS, HEAD, LANES), F32),
                        pltpu.VMEM((2, 3, 2, SUBLANES * HEAD, 2 * LANES), F32),
                        pltpu.VMEM((2, 2, SUBLANES * HEAD, 2 * LANES), BF16),
                        pltpu.VMEM((1, 2, SUBLANES * HEAD, 2 * LANES), BF16)]
                       + (exchange.scratch() if exchange else []),
        compiler_params=_params("arbitrary"),
    )(r, w, k, a, b, v, dy, st, st, *ex_in)
    return list(res[:6]), list(res[6:])


def _rwkv_post_math(y, r, k, v, gate, lw, lb, rk, bd):
    mean = _head_sum(y, bd) * (1.0 / HEAD)
    yc = y - mean
    var = _head_sum(yc * yc, bd) * (1.0 / HEAD)
    rstd = lax.rsqrt(var + LNX_EPS)
    yn = yc * rstd
    rkk = _head_sum(r * k * rk, bd)
    sg = _sigmoid(gate)
    pre = yn * lw + lb + rkk * v
    return yn, rstd, rkk, sg, pre


def _rwkv_prep_bwd(u_a, h_t, grads, mu, wl, w0, a0, kkw, kaw, tm=256):
    s = u_a.shape[0]
    nb = s // tm
    d = h_t.shape[0]

    def body(ua_ref, prev_ref, ht_ref, drs_ref, dws_ref, dks_ref, dvs_ref, das_ref, dbs_ref, drb_ref, dkb_ref, dvb_ref,
             dgt_ref, mu_ref, wl_ref, w0_ref, a0_ref, kkw_ref, kaw_ref,
             du_ref, dwa_ref, dmu_ref, dwl_ref, dw0_ref, da0_ref, dkkw_ref, dkaw_ref, carry):
        i = pl.program_id(0)

        @pl.when(i == 0)
        def _():
            carry[...] = jnp.zeros_like(carry)
            for ref in (dwa_ref, dmu_ref, dwl_ref, dw0_ref, da0_ref, dkkw_ref, dkaw_ref):
                ref[...] = jnp.zeros_like(ref)

        bd = _head_ones()
        mu_v, wl_v, kkw_v, kaw_v = mu_ref[...], wl_ref[...], kkw_ref[...], kaw_ref[...]
        f = _rwkv_elementwise(ua_ref[...], prev_ref[7:8, :], i == nb - 1, mu_v, wl_v, w0_ref[...],
                              a0_ref[...], kkw_v, kaw_v, bd)
        a, kk, k0 = f["a"], f["kk"], f["k0"]
        dk = dks_ref[...] + dkb_ref[...]
        dbs = dbs_ref[...]
        dkk = dbs * a - das_ref[...]
        da = dbs * kk + dk * k0 * kaw_v
        dk0 = dk * (1.0 + (a - 1.0) * kaw_v)
        dkaw_ref[...] += _colsum(dk * k0 * (a - 1.0))
        inv = 1.0 / f["nrm"]
        proj = _head_sum(dkk * kk, bd)
        dkk0 = jnp.where(f["ss"] > 1e-24, (dkk - kk * proj) * inv, dkk * inv)
        dk0 = dk0 + dkk0 * kkw_v
        dkkw_ref[...] += _colsum(dkk0 * k0)
        dza = da * a * (1.0 - a)
        da0_ref[...] += _colsum(dza)
        dz = -dws_ref[...] * f["dec"] * f["e"] * (1.0 - f["sz"])
        dw0_ref[...] += _colsum(dz)
        dll = jnp.concatenate([dz, dza], axis=1).astype(BF16)
        dwl_ref[...] += _dot_tn(f["lin"].astype(BF16), dll)
        dlin = _dot_nt(dll, wl_v)
        lane = lax.broadcasted_iota(jnp.int32, (1, LANES), 1)
        th = f["th"]
        dlo = jnp.where(lane < LORA, dlin * (1.0 - th * th), dlin)
        dus = jnp.concatenate([drs_ref[...] + drb_ref[...], dk0, dvs_ref[...] + dvb_ref[...], dlo, dgt_ref[...]],
                              axis=1)
        dmu_ref[...] += _colsum(dus * f["delta"])
        g1 = dus * mu_v
        rows = lax.broadcasted_iota(jnp.int32, (tm, 1), 0)
        up = jnp.where(rows == tm - 1, carry[...], pltpu.roll(g1, tm - 1, 0))
        dua = dus - g1 + up
        du_ref[...] = dua
        dwa_ref[...] += jnp.dot(ht_ref[...], dua.astype(BF16), preferred_element_type=F32)
        carry[...] = g1[0:1, :]

    rev = lambda w: pl.BlockSpec((tm, w), lambda i: (nb - 1 - i, 0))
    vec = lambda w: pl.BlockSpec((1, w), lambda i: (0, 0))
    wl_spec = pl.BlockSpec((LANES, 2 * D_HALF), lambda i: (0, 0))
    return pl.pallas_call(
        body, name="rwkv_prep_bwd", grid=(nb,),
        in_specs=[rev(SEC), pl.BlockSpec((8, SEC), lambda i: (jnp.maximum((nb - 1 - i) * (tm // 8) - 1, 0), 0)),
                  pl.BlockSpec((d, tm), lambda i: (0, nb - 1 - i))]
                 + [rev(D_HALF)] * 10 + [vec(SEC), wl_spec] + [vec(D_HALF)] * 4,
        out_specs=[rev(SEC), pl.BlockSpec((d, SEC), lambda i: (0, 0)), vec(SEC), wl_spec] + [vec(D_HALF)] * 4,
        out_shape=[jax.ShapeDtypeStruct((s, SEC), F32), jax.ShapeDtypeStruct((d, SEC), F32),
                   jax.ShapeDtypeStruct((1, SEC), F32),
                   jax.ShapeDtypeStruct((LANES, 2 * D_HALF), F32)] + [jax.ShapeDtypeStruct((1, D_HALF), F32)] * 4,
        scratch_shapes=[pltpu.VMEM((1, SEC), F32)],
        compiler_params=_params("arbitrary"),
    )(u_a, u_a, h_t, *grads, mu, wl, w0, a0, kkw, kaw)


def _tri(tm, lower):
    r = lax.broadcasted_iota(jnp.int32, (tm, tm), 0)
    c = lax.broadcasted_iota(jnp.int32, (tm, tm), 1)
    return ((r >= c) if lower else (r <= c)).astype(BF16)


def _head_rms(x, g, bd):
    rinv = lax.rsqrt(_head_sum(x * x, bd) * (1.0 / HEAD) + RMS_EPS)
    xh = x * rinv
    return xh, rinv, xh * g


def _fox_front(h, w_b, fb, qg, kg, tm=256):
    s, d = h.shape

    def body(h_ref, wb_ref, fb_ref, qg_ref, kg_ref, ub_ref, q_ref, k_ref, v_ref, cc_ref, cr_ref, carry):
        i = pl.program_id(0)

        @pl.when(i == 0)
        def _():
            carry[...] = jnp.zeros_like(carry)

        ub_ref[...] = jnp.dot(h_ref[...], wb_ref[...], preferred_element_type=F32)
        bd = _head_ones()
        _, _, qn = _head_rms(ub_ref[:, 0:512], qg_ref[...], bd)
        _, _, kn = _head_rms(ub_ref[:, 512:1024], kg_ref[...], bd)
        q_ref[...] = (qn * ATT_SCALE).astype(BF16)
        k_ref[...] = kn.astype(BF16)
        v_ref[...] = ub_ref[:, 1024:1536].astype(BF16)
        lane = lax.broadcasted_iota(jnp.int32, (1, LANES), 1)
        logf = jnp.where(lane < N_HEADS, _log_sigmoid(ub_ref[:, 2048:2176] + fb_ref[...]), 0.0)
        cum = _exact_dot(logf, _tri(tm, True), ones_first=True) + carry[...]
        for h in range(N_HEADS):
            cc_ref[h] = jnp.broadcast_to(cum[:, h:h + 1], (tm, LANES))
        cr_ref[...] = jnp.transpose(cum)[0:N_HEADS, :]
        carry[...] = cum[tm - 1:tm, :]

    blk = pl.BlockSpec((tm, D_HALF), lambda i: (i, 0))
    return pl.pallas_call(
        body, name="fox_front", grid=(s // tm,),
        in_specs=[pl.BlockSpec((tm, d), lambda i: (i, 0)),
                  pl.BlockSpec(w_b.shape, lambda i: (0, 0), pipeline_mode=pl.Buffered(1)),
                  pl.BlockSpec((1, LANES), lambda i: (0, 0)),
                  pl.BlockSpec((1, D_HALF), lambda i: (0, 0)), pl.BlockSpec((1, D_HALF), lambda i: (0, 0))],
        out_specs=[pl.BlockSpec((tm, SEC), lambda i: (i, 0)), blk, blk, blk,
                   pl.BlockSpec((N_HEADS, tm, LANES), lambda i: (0, i, 0)), pl.BlockSpec((N_HEADS, tm), lambda i: (0, i))],
        out_shape=[jax.ShapeDtypeStruct((s, SEC), F32)] + [jax.ShapeDtypeStruct((s, D_HALF), BF16)] * 3
                  + [jax.ShapeDtypeStruct((N_HEADS, s, LANES), F32), jax.ShapeDtypeStruct((N_HEADS, s), F32)],
        scratch_shapes=[pltpu.VMEM((1, LANES), F32)],
        compiler_params=_params("arbitrary"),
    )(h, w_b, fb, qg, kg)


ATT_T = 256


def _tiles(nblk, by_query):
    if by_query:
        pairs = [(i, j) for i in range(nblk) for j in range(i + 1)]
    else:
        pairs = [(i, j) for j in range(nblk) for i in range(j, nblk)]
    return (jnp.asarray([p[0] for p in pairs], jnp.int32), jnp.asarray([p[1] for p in pairs], jnp.int32))


def _attn_fwd(q, k, v, cc, cr):
    s = q.shape[0]
    t = ATT_T
    nblk = s // t

    def body(qi_ref, kj_ref, q_ref, k_ref, v_ref, cc_ref, cr_ref, o_ref, lse_ref, m_sc, l_sc, acc_sc):
        i = qi_ref[pl.program_id(0)]
        j = kj_ref[pl.program_id(0)]

        @pl.when(j == 0)
        def _():
            m_sc[...] = jnp.full_like(m_sc, NEG)
            l_sc[...] = jnp.zeros_like(l_sc)
            acc_sc[...] = jnp.zeros_like(acc_sc)

        def tile(on_diagonal):
            causal = _causal_tile(t) if on_diagonal else None
            left = lax.broadcasted_iota(jnp.int32, (1, LANES), 1) < HEAD
            for p in range(N_PAIRS):
                lanes = slice(p * LANES, (p + 1) * LANES)
                q2, k2, v2 = q_ref[:, lanes], k_ref[:, lanes], v_ref[:, lanes]
                acc2 = acc_sc[:, lanes]
                for e in range(2):
                    h = 2 * p + e
                    msk = left if e == 0 else jnp.logical_not(left)
                    sc = _dot_nt(jnp.where(msk, q2, jnp.zeros_like(q2)), k2)
                    sc = sc + (_wide(cc_ref[h]) - cr_ref[h:h + 1, :])
                    if on_diagonal:
                        sc = jnp.where(causal, sc, NEG)
                    m_prev = m_sc[h]
                    m_new = jnp.maximum(m_prev, jnp.max(sc, axis=1, keepdims=True))
                    alpha = jnp.exp(m_prev - m_new)
                    pm = jnp.exp(sc - _wide(m_new))
                    l_sc[h] = alpha * l_sc[h] + jnp.sum(pm, axis=1, keepdims=True)
                    m_sc[h] = m_new
                    pv = jnp.dot(pm.astype(BF16), v2, preferred_element_type=F32)
                    acc2 = jnp.where(msk, alpha * acc2 + pv, acc2)
                acc_sc[:, lanes] = acc2

        pl.when(j < i)(functools.partial(tile, False))
        pl.when(j == i)(functools.partial(tile, True))

        @pl.when(j == i)
        def _():
            left = lax.broadcasted_iota(jnp.int32, (1, LANES), 1) < HEAD
            for p in range(N_PAIRS):
                lanes = slice(p * LANES, (p + 1) * LANES)
                inv = jnp.where(left, 1.0 / l_sc[2 * p], 1.0 / l_sc[2 * p + 1])
                o_ref[:, lanes] = acc_sc[:, lanes] * inv
            for h in range(N_HEADS):
                lse_ref[h] = m_sc[h] + jnp.log(l_sc[h])

    qi, kj = _tiles(nblk, by_query=True)
    qblk = pl.BlockSpec((t, D_HALF), lambda n, qi, kj: (qi[n], 0))
    kblk = pl.BlockSpec((t, D_HALF), lambda n, qi, kj: (kj[n], 0))
    qrep = pl.BlockSpec((N_HEADS, t, LANES), lambda n, qi, kj: (0, qi[n], 0))
    return pl.pallas_call(
        body, name="fox_attn_fwd",
        grid_spec=pltpu.PrefetchScalarGridSpec(
            num_scalar_prefetch=2, grid=(qi.shape[0],),
            in_specs=[qblk, kblk, kblk, qrep, pl.BlockSpec((N_HEADS, t), lambda n, qi, kj: (0, kj[n]))],
            out_specs=[qblk, qrep],
            scratch_shapes=[pltpu.VMEM((N_HEADS, t, LANES), F32), pltpu.VMEM((N_HEADS, t, LANES), F32),
                            pltpu.VMEM((t, D_HALF), F32)]),
        out_shape=[jax.ShapeDtypeStruct((s, D_HALF), F32), jax.ShapeDtypeStruct((N_HEADS, s, LANES), F32)],
        compiler_params=_params("arbitrary"),
    )(qi, kj, q, k, v, cc, cr)


def _causal_tile(t):
    return lax.broadcasted_iota(jnp.int32, (t, t), 0) >= lax.broadcasted_iota(jnp.int32, (t, t), 1)


def _wide(x):
    return jnp.concatenate([x, x], axis=1)


def _attn_probs(q2, k2, v2, do2, msk, causal, bias, lse_rows):
    zero = jnp.zeros_like(q2)
    qh = jnp.where(msk, q2, zero)
    doh = jnp.where(msk, do2, zero)
    sc = _dot_nt(qh, k2) + bias
    if causal is not None:
        sc = jnp.where(causal, sc, NEG)
    pm = jnp.exp(sc - _wide(lse_rows))
    dp = _dot_nt(doh, v2)
    return qh, doh, pm, dp


def _attn_bwd_rowdot(q, k, v, do, lse, cc, cr):
    s = q.shape[0]
    t = ATT_T
    nblk = s // t

    def body(qi_ref, kj_ref, q_ref, k_ref, v_ref, do_ref, lse_ref, cc_ref, cr_ref, dd_ref, acc):
        i = qi_ref[pl.program_id(0)]
        j = kj_ref[pl.program_id(0)]

        @pl.when(j == 0)
        def _():
            acc[...] = jnp.zeros_like(acc)

        def tile(on_diagonal):
            causal = _causal_tile(t) if on_diagonal else None
            left = lax.broadcasted_iota(jnp.int32, (1, LANES), 1) < HEAD
            for p in range(N_PAIRS):
                lanes = slice(p * LANES, (p + 1) * LANES)
                q2, k2, v2, do2 = q_ref[:, lanes], k_ref[:, lanes], v_ref[:, lanes], do_ref[:, lanes]
                for e in range(2):
                    h = 2 * p + e
                    msk = left if e == 0 else jnp.logical_not(left)
                    bias = _wide(cc_ref[h]) - cr_ref[h:h + 1, :]
                    _, _, pm, dp = _attn_probs(q2, k2, v2, do2, msk, causal, bias, lse_ref[h])
                    acc[h] += jnp.sum(pm * dp, axis=1, keepdims=True)

        pl.when(j < i)(functools.partial(tile, False))
        pl.when(j == i)(functools.partial(tile, True))

        @pl.when(j == i)
        def _():
            dd_ref[...] = acc[...]

    qi, kj = _tiles(nblk, by_query=True)
    qblk = pl.BlockSpec((t, D_HALF), lambda n, qi, kj: (qi[n], 0))
    qcol = pl.BlockSpec((N_HEADS, t, LANES), lambda n, qi, kj: (0, qi[n], 0))
    kblk = pl.BlockSpec((t, D_HALF), lambda n, qi, kj: (kj[n], 0))
    return pl.pallas_call(
        body, name="fox_attn_rowdot",
        grid_spec=pltpu.PrefetchScalarGridSpec(
            num_scalar_prefetch=2, grid=(qi.shape[0],),
            in_specs=[qblk, kblk, kblk, qblk, qcol, qcol, pl.BlockSpec((N_HEADS, t), lambda n, qi, kj: (0, kj[n]))],
            out_specs=qcol, scratch_shapes=[pltpu.VMEM((N_HEADS, t, LANES), F32)]),
        out_shape=jax.ShapeDtypeStruct((N_HEADS, s, LANES), F32),
        compiler_params=_params("arbitrary"),
    )(qi, kj, q, k, v, do, lse, cc, cr)


def _attn_bwd(q, k, v, do, lse, dd, cc, cr):
    s = q.shape[0]
    t = ATT_T
    nblk = s // t

    def body(qi_ref, kj_ref, q_ref, k_ref, v_ref, do_ref, lse_ref, dd_ref, cc_ref, cr_ref,
             dq_ref, dk_ref, dv_ref, dcr_ref, dk_sc, dv_sc, dcr_sc):
        i = qi_ref[pl.program_id(0)]
        j = kj_ref[pl.program_id(0)]

        @pl.when(pl.program_id(0) == 0)
        def _():
            dq_ref[...] = jnp.zeros_like(dq_ref)

        @pl.when(i == j)
        def _():
            dk_sc[...] = jnp.zeros_like(dk_sc)
            dv_sc[...] = jnp.zeros_like(dv_sc)
            dcr_sc[...] = jnp.zeros_like(dcr_sc)

        def tile(on_diagonal):
            causal = _causal_tile(t) if on_diagonal else None
            left = lax.broadcasted_iota(jnp.int32, (1, LANES), 1) < HEAD
            qrows = pl.ds(pl.multiple_of(i * t, t), t)
            for p in range(N_PAIRS):
                lanes = slice(p * LANES, (p + 1) * LANES)
                q2, k2, v2, do2 = q_ref[:, lanes], k_ref[:, lanes], v_ref[:, lanes], do_ref[:, lanes]
                zero = jnp.zeros_like(q2)
                dq2 = jnp.zeros((t, LANES), F32)
                dk2 = jnp.zeros((t, LANES), F32)
                dv2 = jnp.zeros((t, LANES), F32)
                for e in range(2):
                    h = 2 * p + e
                    msk = left if e == 0 else jnp.logical_not(left)
                    bias = _wide(cc_ref[h]) - cr_ref[h:h + 1, :]
                    qh, doh, pm, dp = _attn_probs(q2, k2, v2, do2, msk, causal, bias, lse_ref[h])
                    dsc = pm * (dp - _wide(dd_ref[h]))
                    dsb = dsc.astype(BF16)
                    dv2 += _dot_tn(pm.astype(BF16), doh)
                    dk2 += _dot_tn(dsb, qh)
                    dq2 += jnp.dot(dsb, jnp.where(msk, k2, zero), preferred_element_type=F32)
                    dcr_sc[h:h + 1, :] += -_colsum(dsc)
                dq_ref[qrows, lanes] += dq2 * ATT_SCALE
                dk_sc[:, lanes] += dk2
                dv_sc[:, lanes] += dv2

        pl.when(i > j)(functools.partial(tile, False))
        pl.when(i == j)(functools.partial(tile, True))

        @pl.when(i == nblk - 1)
        def _():
            dk_ref[...] = dk_sc[...]
            dv_ref[...] = dv_sc[...]
            dcr_ref[...] = dcr_sc[...]

    qi, kj = _tiles(nblk, by_query=False)
    qblk = pl.BlockSpec((t, D_HALF), lambda n, qi, kj: (qi[n], 0))
    qcol = pl.BlockSpec((N_HEADS, t, LANES), lambda n, qi, kj: (0, qi[n], 0))
    kblk = pl.BlockSpec((t, D_HALF), lambda n, qi, kj: (kj[n], 0))
    krow = pl.BlockSpec((N_HEADS, t), lambda n, qi, kj: (0, kj[n]))
    return pl.pallas_call(
        body, name="fox_attn_bwd",
        grid_spec=pltpu.PrefetchScalarGridSpec(
            num_scalar_prefetch=2, grid=(qi.shape[0],),
            in_specs=[qblk, kblk, kblk, qblk, qcol, qcol, qcol, krow],
            out_specs=[pl.BlockSpec((s, D_HALF), lambda n, qi, kj: (0, 0)), kblk, kblk, krow],
            scratch_shapes=[pltpu.VMEM((t, D_HALF), F32), pltpu.VMEM((t, D_HALF), F32), pltpu.VMEM((N_HEADS, t), F32)]),
        out_shape=[jax.ShapeDtypeStruct((s, D_HALF), F32)] * 3 + [jax.ShapeDtypeStruct((N_HEADS, s), F32)],
        compiler_params=_params("arbitrary"),
    )(qi, kj, q, k, v, do, lse, dd, cc, cr)


def _fox_prep_bwd(u_b, h_t, dq, dk, dv, dgate, dcum, fb, qg, kg, tm=256):
    s = u_b.shape[0]
    nb = s // tm
    d = h_t.shape[0]

    def body(ub_ref, ht_ref, dq_ref, dk_ref, dv_ref, dg_ref, dc_ref, fb_ref, qg_ref, kg_ref,
             du_ref, dwb_ref, dqg_ref, dkg_ref, dfb_ref, carry):
        i = pl.program_id(0)

        @pl.when(i == 0)
        def _():
            carry[...] = jnp.zeros_like(carry)
            dwb_ref[...] = jnp.zeros_like(dwb_ref)
            dqg_ref[...] = jnp.zeros_like(dqg_ref)
            dkg_ref[...] = jnp.zeros_like(dkg_ref)
            dfb_ref[...] = jnp.zeros_like(dfb_ref)

        bd = _head_ones()
        for lo, g_ref, d_ref, dgain_ref in ((0, qg_ref, dq_ref, dqg_ref), (512, kg_ref, dk_ref, dkg_ref)):
            gain = g_ref[...]
            xh, rinv, _ = _head_rms(ub_ref[:, lo:lo + 512], gain, bd)
            dn = d_ref[...]
            dgain_ref[...] += _colsum(dn * xh)
            dxh = dn * gain
            du_ref[:, lo:lo + 512] = rinv * (dxh - xh * (_head_sum(dxh * xh, bd) * (1.0 / HEAD)))
        du_ref[:, 1024:1536] = dv_ref[...]
        du_ref[:, 1536:2048] = dg_ref[...]
        lane = lax.broadcasted_iota(jnp.int32, (1, LANES), 1)
        dc = dc_ref[...]
        dlogf = _exact_dot(dc, _tri(tm, False), ones_first=True) + carry[...]
        carry[...] += _colsum(dc)
        fl = ub_ref[:, 2048:2176] + fb_ref[...]
        dfl = jnp.where(lane < N_HEADS, dlogf * (1.0 - _sigmoid(fl)), 0.0)
        du_ref[:, 2048:2176] = dfl
        dfb_ref[...] += _colsum(dfl)
        dwb_ref[...] += jnp.dot(ht_ref[...], du_ref[...].astype(BF16), preferred_element_type=F32)

    rev = lambda w: pl.BlockSpec((tm, w), lambda i: (nb - 1 - i, 0))
    vec = lambda w: pl.BlockSpec((1, w), lambda i: (0, 0))
    return pl.pallas_call(
        body, name="fox_prep_bwd", grid=(nb,),
        in_specs=[rev(SEC), pl.BlockSpec((d, tm), lambda i: (0, nb - 1 - i))] + [rev(D_HALF)] * 4
                 + [rev(LANES), vec(LANES), vec(D_HALF), vec(D_HALF)],
        out_specs=[rev(SEC), pl.BlockSpec((d, SEC), lambda i: (0, 0)), vec(D_HALF), vec(D_HALF), vec(LANES)],
        out_shape=[jax.ShapeDtypeStruct((s, SEC), F32), jax.ShapeDtypeStruct((d, SEC), F32),
                   jax.ShapeDtypeStruct((1, D_HALF), F32), jax.ShapeDtypeStruct((1, D_HALF), F32),
                   jax.ShapeDtypeStruct((1, LANES), F32)],
        scratch_shapes=[pltpu.VMEM((1, LANES), F32)],
        compiler_params=_params("arbitrary"),
    )(u_b, h_t, dq, dk, dv, dgate, dcum, fb, qg, kg)


def _merge(y, r, k, v, gate_a, o, u_b, h, x, tgt, w_g, wa, wb, wo, fg, lw, lb, rk, tm=256):
    s, d = x.shape

    def body(y_ref, r_ref, k_ref, v_ref, ga_ref, o_ref, gb_ref, h_ref, x_ref, t_ref, wg_ref, wa_ref, wb_ref, wo_ref,
             fg_ref, lw_ref, lb_ref, rk_ref,
             dx2_ref, dy_ref, drb_ref, dkb_ref, dvb_ref, dga_ref, do_ref, dgb_ref, dug_ref,
             dwa_ref, dwb_ref, dwo_ref, dfg_ref, loss_ref, dlw_ref, dlb_ref, drk_ref):
        i = pl.program_id(0)

        @pl.when(i == 0)
        def _():
            for ref in (dwa_ref, dwb_ref, dwo_ref, dfg_ref, loss_ref, dlw_ref, dlb_ref, drk_ref):
                ref[...] = jnp.zeros_like(ref)

        bd = _head_ones()
        wa_v, wb_v, wo_v, fg_v = wa_ref[...], wb_ref[...], wo_ref[...], fg_ref[...]
        rv, kv, vv, ga, lw_v, rk_v = r_ref[...], k_ref[...], v_ref[...], ga_ref[...], lw_ref[...], rk_ref[...]
        yn, rstd, rkk, sga, pre = _rwkv_post_math(y_ref[...], rv, kv, vv, ga, lw_v, lb_ref[...], rk_v, bd)
        silu_a = ga * sga
        gb, ov = gb_ref[...], o_ref[...]
        sgb = _sigmoid(gb)
        silu_b = gb * sgb
        ma = (pre * silu_a).astype(BF16)
        mb = (ov * silu_b).astype(BF16)
        ya = jnp.dot(ma, wa_v, preferred_element_type=F32)
        yb = jnp.dot(mb, wb_v, preferred_element_type=F32)
        ug = jnp.dot(h_ref[...], wg_ref[...], preferred_element_type=F32)
        sa = _sigmoid(ug[:, 0:d])
        sb = _sigmoid(ug[:, d:2 * d])
        merged = (sa * ya + sb * yb).astype(BF16)
        x2 = x_ref[...] + jnp.dot(merged, wo_v, preferred_element_type=F32)
        r2 = lax.rsqrt(jnp.mean(x2 * x2, axis=-1, keepdims=True) + RMS_EPS)
        x2h = x2 * r2
        err = x2h * fg_v - t_ref[...]
        loss_ref[...] += _colsum(err * err)
        dyo = err * (1.0 / d)
        dfg_ref[...] += _colsum(dyo * x2h)
        dx2h = dyo * fg_v
        dx2 = r2 * (dx2h - x2h * jnp.mean(dx2h * x2h, axis=-1, keepdims=True))
        dx2_ref[...] = dx2
        dx2b = dx2.astype(BF16)
        dmerged = _dot_nt(dx2b, wo_v)
        dwo_ref[...] += _dot_tn(merged, dx2b)
        dya = dmerged * sa
        dyb = dmerged * sb
        dug_ref[:, 0:d] = dya * ya * (1.0 - sa)
        dug_ref[:, d:2 * d] = dyb * yb * (1.0 - sb)
        dyab = dya.astype(BF16)
        dybb = dyb.astype(BF16)
        dwa_ref[...] += _dot_tn(ma, dyab)
        dwb_ref[...] += _dot_tn(mb, dybb)
        dmb = _dot_nt(dybb, wb_v)
        do_ref[...] = (dmb * silu_b).astype(BF16)
        dgb_ref[...] = dmb * ov * (sgb * (1.0 + gb * (1.0 - sgb)))
        dma = _dot_nt(dyab, wa_v)
        dga_ref[...] = dma * pre * (sga * (1.0 + ga * (1.0 - sga)))
        dpre = dma * silu_a
        dlw_ref[...] += _colsum(dpre * yn)
        dlb_ref[...] += _colsum(dpre)
        dyn = dpre * lw_v
        m1 = _head_sum(dyn, bd) * (1.0 / HEAD)
        m2 = _head_sum(dyn * yn, bd) * (1.0 / HEAD)
        dy_ref[...] = rstd * (dyn - m1 - yn * m2)
        dvb_ref[...] = dpre * rkk
        drkk = _head_sum(dpre * vv, bd)
        drb_ref[...] = drkk * kv * rk_v
        dkb_ref[...] = drkk * rv * rk_v
        drk_ref[...] += _colsum(drkk * rv * kv)

    row = lambda w: pl.BlockSpec((tm, w), lambda i: (i, 0))
    full = lambda a: pl.BlockSpec(a.shape, lambda i: (0, 0))
    once = lambda a: pl.BlockSpec(a.shape, lambda i: (0, 0), pipeline_mode=pl.Buffered(1))
    half = jax.ShapeDtypeStruct((s, D_HALF), F32)
    fshape = lambda a: jax.ShapeDtypeStruct(a.shape, F32)
    return pl.pallas_call(
        body, name="merge_fwd_bwd", grid=(s // tm,),
        in_specs=[row(D_HALF)] * 6 + [pl.BlockSpec((tm, D_HALF), lambda i: (i, 3)), row(d), row(d), row(d),
                                      once(w_g), once(wa), once(wb), once(wo), full(fg), full(lw), full(lb), full(rk)],
        out_specs=[row(d)] + [row(D_HALF)] * 7 + [row(GATE_COLS), full(wa), full(wb), full(wo), full(fg), full(fg),
                                                   full(lw), full(lb), full(rk)],
        out_shape=[jax.ShapeDtypeStruct((s, d), F32)] + [half] * 5 + [jax.ShapeDtypeStruct((s, D_HALF), BF16), half,
                                                                    jax.ShapeDtypeStruct((s, GATE_COLS), F32),
                                                                    fshape(wa), fshape(wb), fshape(wo), fshape(fg),
                                                                    fshape(fg), fshape(lw), fshape(lb), fshape(rk)],
        compiler_params=_params("arbitrary"),
    )(y, r, k, v, gate_a, o, u_b, h, x, tgt, w_g, wa, wb, wo, fg, lw, lb, rk)


def _lora_weight(w_up, a_up):
    z = jnp.zeros((LORA, D_HALF), w_up.dtype)
    return jnp.concatenate([jnp.concatenate([w_up, z], axis=1), jnp.concatenate([z, a_up], axis=1)], axis=0)


def _device_grads(x, tgt, p, w_a, w_up, a_up, late_weights, fwd_exchange=None, bwd_exchange=None, tail_exchange=None):
    wl = _lora_weight(w_up, a_up)
    rk = p["r_k"].reshape(1, D_HALF)
    fb = jnp.pad(p["f_bias"], ((0, 0), (0, LANES - N_HEADS)))
    qg = jnp.tile(p["q_norm_g"], (1, N_HEADS))
    kg = jnp.tile(p["k_norm_g"], (1, N_HEADS))
    fg = p["final_norm_g"].reshape(1, D_MODEL)
    mixer = (p["shift_mu"], wl, p["w0"], p["a0"], p["k_k"], p["k_a"])

    h, u_a, r, dec, k, v, av, bv, gate_a = _rwkv_front(x, p["norm_g"], w_a, *mixer)
    y, st, arrived = _wkv_fwd(r, dec, k, av, bv, v, fwd_exchange)

    w_b, w_g, w_out_a, w_out_b, w_out = late_weights(arrived)
    u_b, q, kn, vb, cc, cr = _fox_front(h, w_b, fb, qg, kg)
    o, lse = _attn_fwd(q, kn, vb, cc, cr)

    (dx2, dy, dr_b, dk_b, dv_b, dgate_a, do, dgate_b, du_g, dwa, dwb, dwo, dfg, loss_vec, dlw, dlb, drk) = _merge(
        y, r, k, v, gate_a, o, u_b, h, x, tgt, w_g, w_out_a, w_out_b, w_out, fg, p["lnx_w"], p["lnx_b"], rk)

    dd = _attn_bwd_rowdot(q, kn, vb, do, lse, cc, cr)
    dq, dk_att, dv_att, dcr = _attn_bwd(q, kn, vb, do, lse, dd, cc, cr)
    dcum = jnp.pad(dcr.T, ((0, 0), (0, LANES - N_HEADS)))
    h_t = h.T
    du_b, dw_b, dqg, dkg, dfb = _fox_prep_bwd(u_b, h_t, dq, dk_att, dv_att, dgate_b, dcum, fb, qg, kg)
    dw_g = _matmul_tn_acc(h_t, du_g, "dw_gate")

    scan_grads, sent = _wkv_bwd(r, dec, k, av, bv, v, dy, st,
                                bwd_exchange(dw_b, dw_g, dwa, dwb, dwo) if bwd_exchange else None)
    du_a, dw_a, dmu, dwl, dw0, da0, dkkw, dkaw = _rwkv_prep_bwd(
        u_a, h_t, (*scan_grads, dr_b, dk_b, dv_b, dgate_a), *mixer)
    dw_up, da_up = dwl[:LORA, :D_HALF], dwl[LORA:, D_HALF:]
    sent_last = _run_on_sequencer(tail_exchange(dw_a, dw_up, da_up), "scatter_tail", 1) if tail_exchange else []
    grad_x, dnorm_g, _ = _inproj_bwd(du_a, du_b, du_g, w_a, w_b, w_g, x, dx2, p["norm_g"])

    grads = dict(
        norm_g=dnorm_g, w_in=(dw_a, dw_b, dw_g), shift_mu=dmu,
        w_lora_up=dw_up, w0=dw0, a_lora_up=da_up, a0=da0, k_k=dkkw, k_a=dkaw,
        r_k=drk.reshape(1, N_HEADS, HEAD), lnx_w=dlw, lnx_b=dlb, f_bias=dfb[:, :N_HEADS],
        q_norm_g=dqg.reshape(N_HEADS, HEAD).sum(axis=0, keepdims=True),
        k_norm_g=dkg.reshape(N_HEADS, HEAD).sum(axis=0, keepdims=True),
        w_out_a=dwa, w_out_b=dwb, w_out=dwo, final_norm_g=dfg.reshape(D_MODEL))
    return loss_vec, grad_x, grads, sent, sent_last


CHIP_FLIPS = ((1, 0), (0, 1), (1, 1))
ANY = pl.BlockSpec(memory_space=pl.ANY)


def _position():
    return lax.axis_index("x"), lax.axis_index("y"), lax.axis_index("c")


def _flip(v, f):
    return 1 - v if f else v


def _both(a, b):
    if a is None:
        return b
    return a if b is None else jnp.logical_and(a, b)


def _when(cond, fn):
    if cond is None:
        fn()
    else:
        pl.when(cond)(fn)


class _Moves:
    def __init__(self, send_sems, recv_sems, local_sems):
        self.send_sems, self.recv_sems, self.local_sems = send_sems, recv_sems, local_sems
        self.remote, self.local = [], []

    def send(self, src, dst, peer, landing, send_if=None, recv_if=None, first=False):
        k = len(self.remote)
        sems = dict(send_sem=self.send_sems.at[k], recv_sem=self.recv_sems.at[k], device_id=peer, device_id_type=MESH)
        out = pltpu.make_async_remote_copy(src_ref=src, dst_ref=dst, **sems)
        arrival = pltpu.make_async_remote_copy(src_ref=src, dst_ref=landing, **sems)
        self.remote.append((out, arrival, send_if, recv_if, first))

    def copy(self, src, dst, cond=None):
        cp = pltpu.make_async_copy(src, dst, self.local_sems.at[len(self.local)])
        self.local.append((cp, cond))

    def start(self, also=None):
        for cp, cond in self.local:
            _when(_both(also, cond), cp.start)
        for out, _, send_if, _, _ in self.remote:
            _when(_both(also, send_if), out.start)

    def wait_arrivals(self, also=None, first=None):
        for _, arrival, _, recv_if, is_first in self.remote:
            if first is None or first == is_first:
                _when(_both(also, recv_if), arrival.wait_recv)

    def wait_sent(self, also=None):
        for out, _, send_if, _, _ in self.remote:
            _when(_both(also, send_if), out.wait_send)
        for cp, cond in self.local:
            _when(_both(also, cond), cp.wait)

    def wait(self, also=None):
        self.wait_arrivals(also)
        self.wait_sent(also)


class _Exchange:
    def __init__(self, operands, out_shapes, n_remote, n_local, build, relays=None, in_place=(), n_staging=0):
        self.operands, self.out_shapes = list(operands), list(out_shapes)
        self.n_remote, self.n_local, self.build = n_remote, n_local, build
        self.relays, self.in_place = relays, in_place
        self.n_staging = n_staging

    def scratch(self):
        return [pltpu.SemaphoreType.DMA((self.n_remote,)), pltpu.SemaphoreType.DMA((self.n_remote,)),
                pltpu.SemaphoreType.DMA((max(self.n_local, 1),))]

    def moves(self, in_refs, out_refs, sems):
        mv = _Moves(*sems)
        self.build(mv, in_refs, out_refs)
        return mv


def _run_on_sequencer(exchange, name, collective_id):
    ins = [jax.new_ref(a, memory_space=pltpu.MemorySpace.HBM) for a in exchange.operands]
    outs = [ins[i] if i in exchange.in_place else jax.empty_ref(s, memory_space=pltpu.MemorySpace.HBM)
            for i, s in enumerate(exchange.out_shapes)]
    forward, to_sibling = exchange.relays or (None, None)
    relay_scratch = [pltpu.SemaphoreType.DMA((stage[0],)) for stage in (forward, to_sibling) if stage for _ in range(2)]

    def launch(*sems):
        x, y, c = _position()
        peers = [(_flip(x, fx), _flip(y, fy), c) for fx, fy in CHIP_FLIPS] + ([(x, y, 1 - c)] if to_sibling else [])
        barrier = pltpu.get_barrier_semaphore()
        for peer in peers:
            pl.semaphore_signal(barrier, inc=1, device_id=peer, device_id_type=MESH)
        pl.semaphore_wait(barrier, len(peers))
        moves = exchange.moves(ins, outs, sems[:3])
        moves.start()
        later = []
        if forward:
            onward = _Moves(sems[3], sems[4], None)
            forward[1](onward, ins, outs)
            moves.wait_arrivals(first=True)
            onward.start()
            moves.wait_arrivals(first=False)
            onward.wait_arrivals()
            later.append(onward)
        else:
            moves.wait_arrivals()
        if to_sibling:
            passed = _Moves(*sems[-2:], None)
            to_sibling[1](passed, ins, outs)
            passed.start()
            passed.wait_arrivals()
            later.append(passed)
        for mv in later + [moves]:
            mv.wait_sent()

    pl.kernel(launch, mesh=plsc.ScalarSubcoreMesh(axis_name="sequencer", num_cores=1), name=name,
              scratch_types=tuple(exchange.scratch() + relay_scratch),
              compiler_params=pltpu.CompilerParams(collective_id=collective_id))()
    return [o[...] for o in outs[:len(outs) - exchange.n_staging]]


def _row_major_copy(a, name):
    r, c = a.shape
    tr = _row_tile(r)

    def body(a_ref, o_ref):
        o_ref[...] = a_ref[...]

    blk = pl.BlockSpec((tr, c), lambda i: (i, 0))
    return pl.pallas_call(body, name=name, grid=(r // tr,), in_specs=[blk], out_specs=blk,
                          out_shape=jax.ShapeDtypeStruct(a.shape, a.dtype), compiler_params=_params("parallel"))(a)


def _is_chip(x, y, chip):
    return jnp.logical_and(x == chip // 2, y == chip % 2)


def _gather_exchange(from_chip, from_all, split=()):
    n1, n2 = len(from_chip), len(from_all)
    near = CHIP_FLIPS[:2]

    def quarters(t, c, first, count=1):
        n = from_chip[t][1].shape[0] // 4
        return pl.ds((2 * c + first) * n, count * n)

    def build(mv, ins, outs):
        x, y, c = _position()
        me = 2 * x + y
        for t, (chip, _) in enumerate(from_chip):
            if t not in split:
                mv.copy(ins[t], outs[t], cond=_is_chip(x, y, chip))
        for t in range(n2):
            mv.copy(ins[n1 + t], outs[n1 + t].at[me])
        for t in split:
            for first in (True, False):
                for f, (fx, fy) in enumerate(near):
                    px, py = _flip(x, fx), _flip(y, fy)
                    part = quarters(t, c, f if first else 1 - f)
                    mv.send(ins[t].at[part], outs[t].at[part], (px, py, c), landing=outs[t].at[part], first=first,
                            send_if=_is_chip(x, y, from_chip[t][0]), recv_if=_is_chip(px, py, from_chip[t][0]))
        for fx, fy in CHIP_FLIPS:
            px, py = _flip(x, fx), _flip(y, fy)
            peer = (px, py, c)
            for t, (chip, _) in enumerate(from_chip):
                if t not in split:
                    mv.send(ins[t], outs[t], peer, landing=outs[t],
                            send_if=_is_chip(x, y, chip), recv_if=_is_chip(px, py, chip))
            for t in range(n2):
                mv.send(ins[n1 + t], outs[n1 + t].at[me], peer, landing=outs[n1 + t].at[2 * px + py])

    def forward(mv, ins, outs):
        x, y, c = _position()
        for t in split:
            chip = from_chip[t][0]
            for f, (fx, fy) in enumerate(near):
                gx, gy = near[1 - f]
                part = quarters(t, c, f)
                mv.send(outs[t].at[part], outs[t].at[part], (_flip(x, gx), _flip(y, gy), c), landing=outs[t].at[part],
                        send_if=_is_chip(_flip(x, fx), _flip(y, fy), chip), recv_if=_is_chip(1 - x, 1 - y, chip))

    def to_sibling(mv, ins, outs):
        x, y, c = _position()
        for t in split:
            came = jnp.logical_not(_is_chip(x, y, from_chip[t][0]))
            mv.send(outs[t].at[quarters(t, c, 0, 2)], outs[t].at[quarters(t, c, 0, 2)], (x, y, 1 - c),
                    landing=outs[t].at[quarters(t, 1 - c, 0, 2)], send_if=came, recv_if=came)

    arrays = [a for _, a in from_chip] + list(from_all)
    shapes = [jax.ShapeDtypeStruct(a.shape, a.dtype) for _, a in from_chip]
    shapes += [jax.ShapeDtypeStruct((N_CHIPS,) + a.shape, a.dtype) for a in from_all]
    n_remote = len(CHIP_FLIPS) * (n1 - len(split) + n2) + 2 * len(near) * len(split)
    relays = ((len(near) * len(split), forward), (len(split), to_sibling)) if split else None
    return _Exchange(arrays, shapes, n_remote, n1 + n2, build, relays, in_place=split)


def _scatter_exchange(to_chip, to_all, via_neighbours=False):
    n1, n2 = len(to_chip), len(to_all)
    near = CHIP_FLIPS[:2]
    direct = near if via_neighbours else CHIP_FLIPS

    def half(t, g):
        n = to_chip[t][1].shape[0] // 2
        return pl.ds(g * n, n)

    def build(mv, ins, outs):
        x, y, c = _position()
        if via_neighbours:
            for t, (chip, _) in enumerate(to_chip):
                for g, (gx, gy) in enumerate(near):
                    ox, oy = near[1 - g]
                    mv.send(ins[t].at[half(t, g)], outs[n1 + n2 + t], (_flip(x, gx), _flip(y, gy), c),
                            landing=outs[n1 + n2 + t], first=True, send_if=_is_chip(1 - x, 1 - y, chip),
                            recv_if=_is_chip(_flip(x, ox), _flip(y, oy), chip))
        for f, (fx, fy) in enumerate(CHIP_FLIPS):
            px, py = _flip(x, fx), _flip(y, fy)
            peer = (px, py, c)
            if (fx, fy) in direct:
                for t, (chip, _) in enumerate(to_chip):
                    mv.send(ins[t], outs[t].at[f], peer, landing=outs[t].at[f],
                            send_if=_is_chip(px, py, chip), recv_if=_is_chip(x, y, chip))
            for t in range(n2):
                mv.send(ins[n1 + t].at[2 * px + py], outs[n1 + t].at[f], peer, landing=outs[n1 + t].at[f])

    def forward(mv, ins, outs):
        x, y, c = _position()
        for t, (chip, _) in enumerate(to_chip):
            for g in range(len(near)):
                ox, oy = near[1 - g]
                far_slot = outs[t].at[len(near)].at[half(t, g)]
                mv.send(outs[n1 + n2 + t], far_slot, (_flip(x, ox), _flip(y, oy), c), landing=far_slot,
                        send_if=_is_chip(_flip(x, ox), _flip(y, oy), chip), recv_if=_is_chip(x, y, chip))

    arrays = [a for _, a in to_chip] + list(to_all)
    shapes = [jax.ShapeDtypeStruct((len(CHIP_FLIPS),) + a.shape, a.dtype) for _, a in to_chip]
    shapes += [jax.ShapeDtypeStruct((len(CHIP_FLIPS),) + a.shape[1:], a.dtype) for a in to_all]
    if not via_neighbours:
        return _Exchange(arrays, shapes, len(CHIP_FLIPS) * (n1 + n2), 0, build)
    shapes += [jax.ShapeDtypeStruct((a.shape[0] // 2, a.shape[1]), a.dtype) for _, a in to_chip]
    return _Exchange(arrays, shapes, 2 * len(near) * n1 + len(CHIP_FLIPS) * n2, 0, build,
                     relays=((len(near) * n1, forward), None), n_staging=n1)


def _swap_sibling(tensors, name):
    n = len(tensors)

    def body(*refs):
        ins, outs = refs[:n], refs[n:2 * n]
        send_sems, recv_sems = refs[2 * n:]
        x, y, c = _position()
        copies = [pltpu.make_async_remote_copy(
            src_ref=ins[t], dst_ref=outs[t], send_sem=send_sems.at[t], recv_sem=recv_sems.at[t],
            device_id=(x, y, 1 - c), device_id_type=MESH) for t in range(n)]
        for cp in copies:
            cp.start()
        for cp in copies:
            cp.wait_recv()
        for cp in copies:
            cp.wait_send()

    return pl.pallas_call(
        body, name=name, in_specs=[ANY] * n, out_specs=[ANY] * n,
        out_shape=[jax.ShapeDtypeStruct(a.shape, a.dtype) for a in tensors],
        scratch_shapes=[pltpu.SemaphoreType.DMA((n,)), pltpu.SemaphoreType.DMA((n,))],
        compiler_params=pltpu.CompilerParams(has_side_effects=True),
    )(*tensors)


def _pair_halves(g):
    r, cols = g.shape
    half = r // 2

    def body(g_ref, o_ref, mine, theirs, send_sem, recv_sem, local_sem):
        x, y, c = _position()
        away = pltpu.make_async_remote_copy(
            src_ref=g_ref.at[pl.ds((1 - c) * half, half)], dst_ref=theirs, send_sem=send_sem, recv_sem=recv_sem,
            device_id=(x, y, 1 - c), device_id_type=MESH)
        kept = pltpu.make_async_copy(g_ref.at[pl.ds(c * half, half)], mine, local_sem)
        away.start()
        kept.start()
        kept.wait()
        away.wait_recv()
        o_ref[...] = (mine[...] + theirs[...]).astype(BF16)
        away.wait_send()

    return pl.pallas_call(
        body, name="pair_halves", in_specs=[ANY], out_specs=pl.BlockSpec(memory_space=pltpu.VMEM),
        out_shape=jax.ShapeDtypeStruct((half, cols), BF16),
        scratch_shapes=[pltpu.VMEM((half, cols), F32), pltpu.VMEM((half, cols), F32),
                        pltpu.SemaphoreType.DMA(()), pltpu.SemaphoreType.DMA(()), pltpu.SemaphoreType.DMA(())],
        compiler_params=pltpu.CompilerParams(has_side_effects=True, vmem_limit_bytes=VMEM_LIMIT),
    )(g)


def _allreduce_small(slab):
    stages = 3

    def body(x_ref, o_ref, buf, send_sems, recv_sems):
        x, y, c = _position()
        peers = ((1 - x, y, c), (x, 1 - y, c), (x, y, 1 - c))
        o_ref[...] = x_ref[...]
        for k, peer in enumerate(peers):
            cp = pltpu.make_async_remote_copy(src_ref=o_ref, dst_ref=buf.at[k], send_sem=send_sems.at[k],
                                              recv_sem=recv_sems.at[k], device_id=peer, device_id_type=MESH)
            cp.start()
            cp.wait()
            o_ref[...] = o_ref[...] + buf[k]

    return pl.pallas_call(
        body, name="allreduce_small",
        in_specs=[pl.BlockSpec(memory_space=pltpu.VMEM)], out_specs=pl.BlockSpec(memory_space=pltpu.VMEM),
        out_shape=jax.ShapeDtypeStruct(slab.shape, slab.dtype),
        scratch_shapes=[pltpu.VMEM((stages,) + slab.shape, slab.dtype),
                        pltpu.SemaphoreType.DMA((stages,)), pltpu.SemaphoreType.DMA((stages,))],
        compiler_params=pltpu.CompilerParams(has_side_effects=True),
    )(slab)


def _row_tile(r):
    return min(r, 256)


def _sum4(stack, recv, me):
    _, r, c = stack.shape
    tr = _row_tile(r)

    def body(me_ref, own_ref, recv_ref, o_ref):
        o_ref[...] = (((own_ref[...] + recv_ref[0].astype(F32)) + recv_ref[1].astype(F32))
                      + recv_ref[2].astype(F32))

    return pl.pallas_call(
        body, name="sum_partials",
        grid_spec=pltpu.PrefetchScalarGridSpec(
            num_scalar_prefetch=1, grid=(r // tr,),
            in_specs=[pl.BlockSpec((None, tr, c), lambda i, me_ref: (me_ref[0], i, 0)),
                      pl.BlockSpec((len(CHIP_FLIPS), tr, c), lambda i, me_ref: (0, i, 0))],
            out_specs=pl.BlockSpec((tr, c), lambda i, me_ref: (i, 0))),
        out_shape=jax.ShapeDtypeStruct((r, c), F32), compiler_params=_params("parallel"),
    )(me, stack, recv)


def _sum_block(own, recv):
    r, c = own.shape
    tr = _row_tile(r)

    def body(own_ref, recv_ref, o_ref):
        o_ref[...] = (((own_ref[...] + recv_ref[0].astype(F32)) + recv_ref[1].astype(F32))
                      + recv_ref[2].astype(F32))

    return pl.pallas_call(
        body, name="sum_block", grid=(r // tr,),
        in_specs=[pl.BlockSpec((tr, c), lambda i: (i, 0)), pl.BlockSpec((len(CHIP_FLIPS), tr, c), lambda i: (0, i, 0))],
        out_specs=pl.BlockSpec((tr, c), lambda i: (i, 0)),
        out_shape=jax.ShapeDtypeStruct((r, c), F32), compiler_params=_params("parallel"),
    )(own, recv)


def _sum_half(own, recv, core):
    r, c = own.shape
    tr = _row_tile(r // 2)
    per_half = r // 2 // tr

    def body(core_ref, own_ref, recv_ref, o_ref):
        mine = pl.program_id(0) // per_half == core_ref[0]

        @pl.when(mine)
        def _():
            o_ref[...] = (((own_ref[...] + recv_ref[0].astype(F32)) + recv_ref[1].astype(F32))
                          + recv_ref[2].astype(F32))

        @pl.when(jnp.logical_not(mine))
        def _():
            o_ref[...] = own_ref[...]

    return pl.pallas_call(
        body, name="sum_half",
        grid_spec=pltpu.PrefetchScalarGridSpec(
            num_scalar_prefetch=1, grid=(r // tr,),
            in_specs=[pl.BlockSpec((tr, c), lambda i, core: (i, 0)),
                      pl.BlockSpec((len(CHIP_FLIPS), tr, c), lambda i, core: (0, i % per_half, 0))],
            out_specs=pl.BlockSpec((tr, c), lambda i, core: (i, 0))),
        out_shape=jax.ShapeDtypeStruct((r, c), F32), compiler_params=_params("parallel"),
    )(core, own, recv)


def _adamw_math(w, g, m, v):
    m = ADAM_B1 * m + (1.0 - ADAM_B1) * g
    v = ADAM_B2 * v + (1.0 - ADAM_B2) * (g * g)
    m_hat = m / (1.0 - ADAM_B1 ** ADAM_STEP)
    v_hat = v / (1.0 - ADAM_B2 ** ADAM_STEP)
    delta = -ADAM_LR * (m_hat / (jnp.sqrt(v_hat) + ADAM_EPS) + ADAM_WD * w)
    return delta, m, v


def _adamw(w, m, v, g_parts, name):
    r, c = w.shape
    tr = _row_tile(r)
    n = len(g_parts)

    def body(*refs):
        w_ref, m_ref, v_ref = refs[:3]
        g_refs = refs[3:3 + n]
        g_out, d_out, m_out, v_out, zero_out = refs[3 + n:]
        g = g_refs[0][...]
        for ref in g_refs[1:]:
            g = g + ref[...]
        g_out[...] = g
        d_out[...], m_out[...], v_out[...] = _adamw_math(w_ref[...], g, m_ref[...], v_ref[...])
        zero_out[0] = 0

    blk = pl.BlockSpec((tr, c), lambda i: (i, 0))
    return pl.pallas_call(
        body, name=name, grid=(r // tr,), in_specs=[blk] * (3 + n),
        out_specs=[blk] * 4 + [pl.BlockSpec(memory_space=pltpu.SMEM)],
        out_shape=[jax.ShapeDtypeStruct((r, c), F32)] * 4 + [jax.ShapeDtypeStruct((1,), jnp.int32)],
        compiler_params=_params("arbitrary"),
    )(w, m, v, *g_parts)


def _adamw_with_sibling(w, m, v, mine, name):
    r, c = w.shape
    tr = _row_tile(r)
    nb = r // tr

    def body(w_ref, m_ref, v_ref, mine_ref, mine_hbm, g_out, d_out, m_out, v_out, zero_out, theirs, send_sems, recv_sems):
        i = pl.program_id(0)
        x, y, core = _position()

        def tile(k):
            return pltpu.make_async_remote_copy(
                src_ref=mine_hbm.at[pl.ds(pl.multiple_of(k * tr, tr), tr)], dst_ref=theirs.at[k],
                send_sem=send_sems.at[k], recv_sem=recv_sems.at[k], device_id=(x, y, 1 - core), device_id_type=MESH)

        @pl.when(i == 0)
        def _():
            for k in range(nb):
                tile(k).start()

        tile(i).wait_recv()
        g = mine_ref[...] + theirs[i]
        g_out[...] = g
        d_out[...], m_out[...], v_out[...] = _adamw_math(w_ref[...], g, m_ref[...], v_ref[...])
        zero_out[0] = 0

        @pl.when(i == nb - 1)
        def _():
            for k in range(nb):
                tile(k).wait_send()

    blk = pl.BlockSpec((tr, c), lambda i: (i, 0))
    return pl.pallas_call(
        body, name=name, grid=(nb,), in_specs=[blk] * 4 + [ANY],
        out_specs=[blk] * 4 + [pl.BlockSpec(memory_space=pltpu.SMEM)],
        out_shape=[jax.ShapeDtypeStruct((r, c), F32)] * 4 + [jax.ShapeDtypeStruct((1,), jnp.int32)],
        scratch_shapes=[pltpu.VMEM((nb, tr, c), F32), pltpu.SemaphoreType.DMA((nb,)), pltpu.SemaphoreType.DMA((nb,))],
        compiler_params=pltpu.CompilerParams(dimension_semantics=("arbitrary",), has_side_effects=True,
                                             vmem_limit_bytes=VMEM_LIMIT),
    )(w, m, v, mine, mine)


def _adamw_small(total, w, m, v):
    sizes = [w[n].size for n in SMALL]
    flat = lambda d: [d[n].reshape(1, -1) for n in SMALL]
    k = len(SMALL)

    def body(*refs):
        total_ref, w_refs, m_refs, v_refs = refs[0], refs[1:1 + k], refs[1 + k:1 + 2 * k], refs[1 + 2 * k:1 + 3 * k]
        outs = refs[1 + 3 * k:]
        for i, size in enumerate(sizes):
            g = total_ref[i:i + 1, 0:size]
            outs[i][...] = g
            outs[k + i][...], outs[2 * k + i][...], outs[3 * k + i][...] = _adamw_math(
                w_refs[i][...], g, m_refs[i][...], v_refs[i][...])

    res = pl.pallas_call(
        body, name="adamw_small", out_shape=[jax.ShapeDtypeStruct((1, size), F32) for size in sizes] * 4,
        compiler_params=_params(),
    )(total, *flat(w), *flat(m), *flat(v))
    return [{n: res[j * k + i].reshape(w[n].shape) for i, n in enumerate(SMALL)} for j in range(4)]


SHARDED = ("w_in", "w_lora_up", "a_lora_up", "w_out_a", "w_out_b", "w_out")
ROW_SHARDED = ("w_out",)
SMALL = ("norm_g", "shift_mu", "w0", "a0", "k_k", "k_a", "r_k", "lnx_w", "lnx_b", "f_bias", "q_norm_g", "k_norm_g",
         "final_norm_g")
WEIGHTS = ("norm_g", "w_in", "shift_mu", "w_lora_up", "w0", "a_lora_up", "a0", "k_k", "k_a", "r_k", "lnx_w", "lnx_b",
           "f_bias", "q_norm_g", "k_norm_g", "w_out_a", "w_out_b", "w_out", "final_norm_g")
SLAB_ROWS = 16
SLAB_COLS = SEC


def _to_slab(named, extra=None):
    rows = [jnp.pad(named[n].reshape(1, -1), ((0, 0), (0, SLAB_COLS - named[n].size))) for n in SMALL]
    if extra is not None:
        rows.append(jnp.pad(extra.reshape(1, -1), ((0, 0), (0, SLAB_COLS - extra.size))))
    rows.append(jnp.zeros((SLAB_ROWS - len(rows), SLAB_COLS), F32))
    return jnp.concatenate(rows, axis=0)


def _by_chip(g, name):
    if name in ROW_SHARDED:
        return g.reshape(N_CHIPS, g.shape[0] // N_CHIPS, g.shape[1])
    r, c = g.shape
    return g.reshape(r, N_CHIPS, c // N_CHIPS).transpose(1, 0, 2)


def _from_chips(stack, name):
    if name in ROW_SHARDED:
        return stack.reshape(-1, stack.shape[2])
    _, r, c = stack.shape
    return stack.transpose(1, 0, 2).reshape(r, N_CHIPS * c)


def kernel(x, norm_g, w_in, shift_mu, w_lora_up, w0, a_lora_up, a0, k_k, k_a, r_k, lnx_w, lnx_b, f_bias, q_norm_g, k_norm_g, w_out_a, w_out_b, w_out, final_norm_g, loss_target, m_norm_g, m_w_in, m_shift_mu, m_w_lora_up, m_w0, m_a_lora_up, m_a0, m_k_k, m_k_a, m_r_k, m_lnx_w, m_lnx_b, m_f_bias, m_q_norm_g, m_k_norm_g, m_w_out_a, m_w_out_b, m_w_out, m_final_norm_g, v_norm_g, v_w_in, v_shift_mu, v_w_lora_up, v_w0, v_a_lora_up, v_a0, v_k_k, v_k_a, v_r_k, v_lnx_w, v_lnx_b, v_f_bias, v_q_norm_g, v_k_norm_g, v_w_out_a, v_w_out_b, v_w_out, v_final_norm_g):
    w = dict(norm_g=norm_g, w_in=w_in, shift_mu=shift_mu, w_lora_up=w_lora_up, w0=w0, a_lora_up=a_lora_up, a0=a0,
             k_k=k_k, k_a=k_a, r_k=r_k, lnx_w=lnx_w, lnx_b=lnx_b, f_bias=f_bias, q_norm_g=q_norm_g,
             k_norm_g=k_norm_g, w_out_a=w_out_a, w_out_b=w_out_b, w_out=w_out, final_norm_g=final_norm_g)
    m = dict(norm_g=m_norm_g, w_in=m_w_in, shift_mu=m_shift_mu, w_lora_up=m_w_lora_up, w0=m_w0,
             a_lora_up=m_a_lora_up, a0=m_a0, k_k=m_k_k, k_a=m_k_a, r_k=m_r_k, lnx_w=m_lnx_w, lnx_b=m_lnx_b,
             f_bias=m_f_bias, q_norm_g=m_q_norm_g, k_norm_g=m_k_norm_g, w_out_a=m_w_out_a, w_out_b=m_w_out_b,
             w_out=m_w_out, final_norm_g=m_final_norm_g)
    v = dict(norm_g=v_norm_g, w_in=v_w_in, shift_mu=v_shift_mu, w_lora_up=v_w_lora_up, w0=v_w0,
             a_lora_up=v_a_lora_up, a0=v_a0, k_k=v_k_k, k_a=v_k_a, r_k=v_r_k, lnx_w=v_lnx_w, lnx_b=v_lnx_b,
             f_bias=v_f_bias, q_norm_g=v_q_norm_g, k_norm_g=v_k_norm_g, w_out_a=v_w_out_a, w_out_b=v_w_out_b,
             w_out=v_w_out, final_norm_g=v_final_norm_g)
    shapes = {n: w[n].shape for n in WEIGHTS}

    shard = {n: w[n][0].astype(BF16) for n in SHARDED}
    late = ("w_out_a", "w_out_b", "w_out")
    loras = ("w_lora_up", "a_lora_up")
    w_in_head, w_in_tail = shard["w_in"][:, :A_TAIL], shard["w_in"][:, A_TAIL:]
    shard0, shard1_head, up_stack, aup_stack = _run_on_sequencer(_gather_exchange(
        [(0, shard["w_in"]), (1, w_in_head)], [shard[n] for n in loras], split=(0, 1)), "gather_early", 2)
    moments = (_row_major_copy(m["w_in"][0], "m_w_in_rows"), _row_major_copy(v["w_in"][0], "v_w_in_rows"))
    shard0, moments = lax.optimization_barrier((shard0, moments))
    w_a = jnp.concatenate([shard0, shard1_head], axis=1)

    def late_weights(arrived):
        shard1_tail, shard2, shard3 = arrived[:3]
        w_b = jnp.concatenate([shard1_tail, shard2[:, :B_TAIL], jnp.zeros((D_MODEL, SEC - FOX_REAL), BF16)], axis=1)
        w_g = jnp.concatenate([shard2[:, B_TAIL:], shard3], axis=1)
        return (w_b, w_g, *[_from_chips(s, n) for n, s in zip(late, arrived[3:])])

    own = {}
    cut = {"block0": lambda: own["dw_a"][:, :SHARD_COLS], "head1": lambda: own["dw_a"][:, SHARD_COLS:],
           "tail1": lambda: own["dw_b"][:, :B_HEAD],
           "block2": lambda: jnp.concatenate([own["dw_b"][:, B_HEAD:FOX_REAL], own["dw_g"][:, :G_HEAD]], axis=1),
           "block3": lambda: own["dw_g"][:, G_HEAD:]}

    def bwd_exchange(dw_b, dw_g, dwa, dwb, dwo):
        own.update(dw_b=dw_b, dw_g=dw_g)
        own.update({n: _by_chip(g, n) for n, g in zip(late, (dwa, dwb, dwo))})
        return _scatter_exchange([(1, cut["tail1"]().astype(BF16)), (2, cut["block2"]().astype(BF16)),
                                  (3, cut["block3"]().astype(BF16))], [own[n].astype(BF16) for n in late])

    def tail_exchange(dw_a, dw_up, da_up):
        own.update(dw_a=dw_a)
        own.update({n: _by_chip(g, n) for n, g in zip(loras, (dw_up, da_up))})
        pair = _pair_halves(dw_a)
        return _scatter_exchange([(0, pair[:, :SHARD_COLS]), (1, pair[:, SHARD_COLS:])],
                                 [own[n].astype(BF16) for n in loras], via_neighbours=True)

    small = {n: w[n] for n in SMALL}
    loss_vec, grad_x, grads, sent, sent_last = _device_grads(
        x[0], loss_target[0], small, w_a, _from_chips(up_stack, "w_lora_up"), _from_chips(aup_stack, "a_lora_up"),
        late_weights, _gather_exchange([(1, w_in_tail), (2, shard["w_in"]), (3, shard["w_in"])], [shard[n] for n in late]),
        bwd_exchange, tail_exchange)

    total = _allreduce_small(_to_slab(grads, extra=loss_vec))
    loss = (0.5 / D_MODEL) * jnp.sum(total[len(SMALL)])
    out_g, out_d, out_m, out_v = _adamw_small(total, w, m, v)

    xpos, ypos, cpos = _position()
    me = (2 * xpos + ypos).astype(jnp.int32).reshape(1)
    core = cpos.astype(jnp.int32).reshape(1)
    core_sum, theirs = {}, {}

    def update(n):
        if n == "w_in":
            g, d, m2, v2, zero = _adamw_with_sibling(w[n][0], *moments, core_sum[n], "adamw_" + n)
        else:
            g, d, m2, v2, zero = _adamw(w[n][0], m[n][0], v[n][0], [core_sum[n], theirs[n]], "adamw_" + n)
        out_g[n], out_d[n], out_m[n], out_v[n] = (a.reshape(shapes[n]) for a in (g, d, m2, v2))
        return zero

    sent_last, out_d["norm_g"] = lax.optimization_barrier((sent_last, out_d["norm_g"]))
    core_sum["w_in"] = lax.switch(me[0], [
        lambda: _sum_half(cut["block0"](), sent_last[0], core),
        lambda: jnp.concatenate([_sum_half(cut["head1"](), sent_last[1], core), _sum_block(cut["tail1"](), sent[0])],
                                axis=1),
        lambda: _sum_block(cut["block2"](), sent[1]),
        lambda: _sum_block(cut["block3"](), sent[2])])
    core_sum.update({n: _sum4(own[n], r, me) for n, r in zip(loras, sent_last[2:])})
    rest = ("w_in",) + loras
    theirs.update(zip(loras, _swap_sibling([core_sum[n] for n in loras], "swap_sibling")))
    after_w_in = me + [update(n) for n in rest][0]
    core_sum.update({n: _sum4(own[n], r, after_w_in) for n, r in zip(late, sent[3:])})
    theirs.update(zip(late, _swap_sibling([core_sum[n] for n in late], "swap_sibling_late")))
    for n in late:
        update(n)

    return (loss, grad_x.reshape(x.shape), *[out_g[n] for n in WEIGHTS], *[out_d[n] for n in WEIGHTS],
            *[out_m[n] for n in WEIGHTS], *[out_v[n] for n in WEIGHTS])
```

```python
import functools
import math

import jax
import jax.numpy as jnp
from jax import lax
from jax.experimental import pallas as pl
from jax.experimental.pallas import tpu as pltpu
from jax.experimental.pallas import tpu_sc as plsc

F32 = jnp.float32
BF16 = jnp.bfloat16

D_MODEL = 1024
D_HALF = 512
HEAD = 64
N_HEADS = 8
LORA = 64
RWKV_COLS = 2176
FOX_REAL = 2056
SEC = 2176
GATE_COLS = 2048
IN_COLS = 6280
N_CHIPS = 4
SHARD_COLS = IN_COLS // N_CHIPS
A_TAIL = RWKV_COLS - SHARD_COLS
B_HEAD = SHARD_COLS - A_TAIL
B_TAIL = FOX_REAL - B_HEAD
G_HEAD = SHARD_COLS - B_TAIL
RMS_EPS = 1e-6
LNX_EPS = 64e-5
ATT_SCALE = HEAD ** -0.5
NEG = -1e30

ADAM_LR = 0.001
ADAM_B1 = 0.9
ADAM_B2 = 0.999
ADAM_EPS = 1e-08
ADAM_WD = 0.01
ADAM_STEP = 10

LANES = 128
SUBLANES = 8
VMEM_LIMIT = 56 * 1024 * 1024
MESH = pl.DeviceIdType.MESH


def _params(*sem):
    return pltpu.CompilerParams(dimension_semantics=sem if sem else None, vmem_limit_bytes=VMEM_LIMIT)


def _sigmoid(x):
    return 1.0 / (1.0 + jnp.exp(-x))


def _log_sigmoid(x):
    return jnp.minimum(x, 0.0) - jnp.log(1.0 + jnp.exp(-jnp.abs(x)))


def _head_ones():
    r = lax.broadcasted_iota(jnp.int32, (LANES, LANES), 0) >> 6
    c = lax.broadcasted_iota(jnp.int32, (LANES, LANES), 1) >> 6
    return (r == c).astype(BF16)


def _split3(x):
    hi = x.astype(BF16)
    r1 = x - hi.astype(F32)
    mid = r1.astype(BF16)
    lo = (r1 - mid.astype(F32)).astype(BF16)
    return hi, mid, lo


def _exact_dot(x, ones_bf16, ones_first=False):
    out = None
    for piece in _split3(x):
        if ones_first:
            t = jnp.dot(ones_bf16, piece, preferred_element_type=F32)
        else:
            t = jnp.dot(piece, ones_bf16, preferred_element_type=F32)
        out = t if out is None else out + t
    return out


def _head_sum(x, bd):
    n = x.shape[1] // LANES
    parts = [_exact_dot(x[:, i * LANES:(i + 1) * LANES], bd) for i in range(n)]
    return parts[0] if n == 1 else jnp.concatenate(parts, axis=1)


def _dot_nt(a, b):
    return lax.dot_general(a, b, (((1,), (1,)), ((), ())), preferred_element_type=F32)


def _dot_tn(a, b):
    return lax.dot_general(a, b, (((0,), (0,)), ((), ())), preferred_element_type=F32)


def _colsum(x):
    return jnp.sum(x, axis=0, keepdims=True)


def _matmul_tn_acc(at, b, name, tk=512):
    m, k = at.shape
    n = b.shape[1]

    def body(a_ref, b_ref, o_ref):
        j = pl.program_id(0)

        @pl.when(j == 0)
        def _():
            o_ref[...] = jnp.zeros_like(o_ref)

        o_ref[...] += jnp.dot(a_ref[...], b_ref[...].astype(BF16), preferred_element_type=F32)

    return pl.pallas_call(
        body, name=name, grid=(k // tk,),
        in_specs=[pl.BlockSpec((m, tk), lambda j: (0, j)), pl.BlockSpec((tk, n), lambda j: (j, 0))],
        out_specs=pl.BlockSpec((m, n), lambda j: (0, 0)),
        out_shape=jax.ShapeDtypeStruct((m, n), F32), compiler_params=_params("arbitrary"),
    )(at, b)


def _inproj_bwd(du_a, du_b, du_g, w_a, w_b, w_g, x, dx2, g, exchange=None, tm=256):
    s, d = x.shape
    nb = s // tm

    def body(*refs):
        ((da_ref, db_ref, dg_ref, wa_ref, wb_ref, wg_ref, x_ref, dx2_ref, g_ref), (gx_ref, gg_ref), _,
         moves) = _split_refs(refs, 9, 2, exchange)
        i = pl.program_id(0)
        if moves:
            moves.start(also=(i == 0))

        @pl.when(i == 0)
        def _():
            gg_ref[...] = jnp.zeros_like(gg_ref)

        dh = _dot_nt(da_ref[...].astype(BF16), wa_ref[...])
        dh += _dot_nt(db_ref[...].astype(BF16), wb_ref[...])
        dh += _dot_nt(dg_ref[...].astype(BF16), wg_ref[...])
        xv = x_ref[...]
        r = lax.rsqrt(jnp.mean(xv * xv, axis=-1, keepdims=True) + RMS_EPS)
        xh = xv * r
        gg_ref[...] += _colsum(dh * xh)
        dxh = dh * g_ref[...]
        gx_ref[...] = dx2_ref[...] + r * (dxh - xh * jnp.mean(dxh * xh, axis=-1, keepdims=True))
        if moves:
            moves.wait(also=(i == nb - 1))

    row = lambda w: pl.BlockSpec((tm, w), lambda i: (i, 0))
    full = lambda a: pl.BlockSpec(a.shape, lambda i: (0, 0))
    ex_in = exchange.operands if exchange else []
    ex_out = exchange.out_shapes if exchange else []
    res = pl.pallas_call(
        body, name="inproj_bwd", grid=(nb,),
        in_specs=[row(SEC), row(SEC), row(GATE_COLS), full(w_a), full(w_b), full(w_g), row(d), row(d), full(g)]
                 + [ANY] * len(ex_in),
        out_specs=[row(d), pl.BlockSpec((1, d), lambda i: (0, 0))] + [ANY] * len(ex_out),
        out_shape=[jax.ShapeDtypeStruct((s, d), F32), jax.ShapeDtypeStruct((1, d), F32)] + ex_out,
        scratch_shapes=exchange.scratch() if exchange else [],
        compiler_params=_params("arbitrary"),
    )(du_a, du_b, du_g, w_a, w_b, w_g, x, dx2, g, *ex_in)
    return res[0], res[1], list(res[2:])


def _rwkv_elementwise(ua, prev_row, first, mu, wl, w0, a0, kkw, kaw, bd):
    tm = ua.shape[0]
    rows = lax.broadcasted_iota(jnp.int32, (tm, 1), 0)
    prev = jnp.where(first, jnp.zeros_like(prev_row), prev_row)
    shifted = jnp.where(rows == 0, prev, pltpu.roll(ua, 1, 0))
    delta = shifted - ua
    us = ua + delta * mu
    r = us[:, 0:512]
    k0 = us[:, 512:1024]
    v = us[:, 1024:1536]
    lo = us[:, 1536:1664]
    gate = us[:, 1664:2176]
    lane = lax.broadcasted_iota(jnp.int32, (1, LANES), 1)
    th = jnp.tanh(lo)
    lin = jnp.where(lane < LORA, th, lo)
    ll = jnp.dot(lin.astype(BF16), wl, preferred_element_type=F32)
    sz = _sigmoid(w0 + ll[:, :512])
    e = sz * math.exp(-0.5)
    dec = jnp.exp(-e)
    a = _sigmoid(a0 + ll[:, 512:])
    kk0 = k0 * kkw
    ss = _head_sum(kk0 * kk0, bd)
    nrm = jnp.maximum(jnp.sqrt(ss), 1e-12)
    kk = kk0 / nrm
    k = k0 * (1.0 + (a - 1.0) * kaw)
    return dict(delta=delta, us=us, r=r, k0=k0, v=v, lo=lo, gate=gate, th=th, lin=lin, sz=sz, e=e, dec=dec,
                a=a, kk0=kk0, ss=ss, nrm=nrm, kk=kk, k=k)


def _rwkv_front(x, g, w_a, mu, wl, w0, a0, kkw, kaw, tm=256):
    s, d = x.shape

    def body(x_ref, g_ref, wa_ref, mu_ref, wl_ref, w0_ref, a0_ref, kkw_ref, kaw_ref,
             h_ref, ua_ref, r_ref, w_ref, k_ref, v_ref, a_ref, b_ref, gate_ref, last_row):
        i = pl.program_id(0)
        xv = x_ref[...]
        h = (xv * lax.rsqrt(jnp.mean(xv * xv, axis=-1, keepdims=True) + RMS_EPS) * g_ref[...]).astype(BF16)
        h_ref[...] = h
        ua = jnp.dot(h, wa_ref[...], preferred_element_type=F32)
        ua_ref[...] = ua

        @pl.when(i == 0)
        def _():
            last_row[...] = jnp.zeros_like(last_row)

        f = _rwkv_elementwise(ua, last_row[...], i == 0, mu_ref[...], wl_ref[...], w0_ref[...],
                              a0_ref[...], kkw_ref[...], kaw_ref[...], _head_ones())
        last_row[...] = ua[tm - 1:tm, :]
        r_ref[...] = f["r"]
        w_ref[...] = f["dec"]
        k_ref[...] = f["k"]
        v_ref[...] = f["v"]
        a_ref[...] = -f["kk"]
        b_ref[...] = f["kk"] * f["a"]
        gate_ref[...] = f["gate"]

    vec = lambda w: pl.BlockSpec((1, w), lambda i: (0, 0))
    row = lambda w: pl.BlockSpec((tm, w), lambda i: (i, 0))
    return pl.pallas_call(
        body, name="rwkv_front", grid=(s // tm,),
        in_specs=[row(d), vec(d), pl.BlockSpec(w_a.shape, lambda i: (0, 0), pipeline_mode=pl.Buffered(1)),
                  vec(SEC), pl.BlockSpec((LANES, 2 * D_HALF), lambda i: (0, 0)),
                  vec(D_HALF), vec(D_HALF), vec(D_HALF), vec(D_HALF)],
        out_specs=[row(d), row(SEC)] + [row(D_HALF)] * 7,
        out_shape=[jax.ShapeDtypeStruct((s, d), BF16), jax.ShapeDtypeStruct((s, SEC), F32)]
                  + [jax.ShapeDtypeStruct((s, D_HALF), F32)] * 7,
        scratch_shapes=[pltpu.VMEM((1, SEC), F32)],
        compiler_params=_params("arbitrary"),
    )(x, g, w_a, mu, wl, w0, a0, kkw, kaw)


SCAN_TB = 128
N_PAIRS = 4


def _pair_sum(x, left):
    s_l = jnp.sum(jnp.where(left, x, 0.0), axis=1, keepdims=True)
    s_r = jnp.sum(jnp.where(left, 0.0, x), axis=1, keepdims=True)
    return jnp.where(left, s_l, s_r)


def _pair_dot(x, row_l, row_r, left):
    s_l = jnp.sum(x * row_l, axis=1, keepdims=True)
    s_r = jnp.sum(x * row_r, axis=1, keepdims=True)
    return jnp.where(left, s_l, s_r)


def _halves(rows8):
    lane = lax.broadcasted_iota(jnp.int32, rows8.shape, 1)
    keep_left = (lane & (LANES - 1)) < HEAD
    return jnp.where(keep_left, rows8, 0.0), jnp.where(keep_left, 0.0, rows8)


def _quad_consts():
    lane = lax.broadcasted_iota(jnp.int32, (HEAD, 2 * LANES), 1)
    rowi = lax.broadcasted_iota(jnp.int32, (HEAD, 2 * LANES), 0)
    diag2 = rowi == (lane & (HEAD - 1))
    r = lax.broadcasted_iota(jnp.int32, (2 * LANES, 2 * LANES), 0) >> 6
    c = lax.broadcasted_iota(jnp.int32, (2 * LANES, 2 * LANES), 1) >> 6
    return diag2, (r == c).astype(BF16)


def _rows_to_columns(x8, diag2, bd2):
    lhs = jnp.concatenate([jnp.where(diag2, x8[i:i + 1], 0.0).astype(BF16) for i in range(SUBLANES)], axis=0)
    return jnp.dot(lhs, bd2, preferred_element_type=F32)


def _diag_rows(qtile, diag2, bd2, sub_row2):
    res = jnp.dot(qtile, bd2, preferred_element_type=F32)
    out = jnp.zeros((SUBLANES, 2 * LANES), F32)
    for i in range(SUBLANES):
        out = jnp.where(sub_row2 == i, _colsum(jnp.where(diag2, res[i * HEAD:(i + 1) * HEAD], 0.0)), out)
    return out


def _store_tile(qbuf, slot, p, i, x):
    qbuf[slot, p // 2, i * HEAD:(i + 1) * HEAD, (p % 2) * LANES:(p % 2 + 1) * LANES] = x.astype(BF16)


def _left_half():
    return lax.broadcasted_iota(jnp.int32, (HEAD, LANES), 1) < HEAD


def _split_refs(refs, n_rows, n_out, exchange):
    n_in = len(exchange.operands) if exchange else 0
    n_ex_out = len(exchange.out_shapes) if exchange else 0
    refs = list(refs)
    rows, refs = refs[:n_rows], refs[n_rows:]
    ex_in, refs = refs[:n_in], refs[n_in:]
    outs, refs = refs[:n_out], refs[n_out:]
    ex_out, refs = refs[:n_ex_out], refs[n_ex_out:]
    scratch, sems = (refs[:-3], refs[-3:]) if exchange else (refs, None)
    moves = exchange.moves(ex_in, ex_out, sems) if exchange else None
    return rows, outs, scratch, moves


def _wkv_fwd(r, w, k, a, b, v, exchange=None):
    s = r.shape[0]
    tb = SCAN_TB
    nb = s // tb

    def body(*refs):
        (r_ref, w_ref, k_ref, a_ref, b_ref, v_ref), (y_ref, st_ref), (state, vbuf, qbuf), moves = _split_refs(
            refs, 6, 2, exchange)
        g = pl.program_id(0)
        if moves:
            moves.start(also=(g == 0))

        @pl.when(g == 0)
        def _():
            state[...] = jnp.zeros_like(state)
            qbuf[...] = jnp.zeros_like(qbuf)

        left = _left_half()
        diag2, bd2 = _quad_consts()
        sub_row2 = lax.broadcasted_iota(jnp.int32, (SUBLANES, 2 * LANES), 0)
        groups = tb // SUBLANES
        quads = [slice(g2 * 2 * LANES, (g2 + 1) * 2 * LANES) for g2 in range(2)]

        def rows_of(q):
            return pl.ds(pl.multiple_of(q * SUBLANES, SUBLANES), SUBLANES)

        def v_tiles(q, slot):
            v8 = v_ref[rows_of(q), :]
            for g2 in range(2):
                vbuf[slot, g2] = _rows_to_columns(v8[:, quads[g2]], diag2, bd2)

        def chain(q, slot):
            rows8 = rows_of(q)
            a8, w8, b8, k8, r8 = (x[rows8, :] for x in (a_ref, w_ref, b_ref, k_ref, r_ref))
            pairs = [slice(p * LANES, (p + 1) * LANES) for p in range(N_PAIRS)]
            a_next = pltpu.roll(a8, SUBLANES - 1, 0)
            (a8_l, a8_r), (wa8_l, wa8_r) = _halves(a8), _halves(w8 * a_next)
            ba8 =jnp.concatenate([_pair_sum(b8[:, pr] * a_next[:, pr], left[0:SUBLANES]) for pr in pairs], axis=1)
            ka8 = jnp.concatenate([_pair_sum(k8[:, pr] * a_next[:, pr], left[0:SUBLANES]) for pr in pairs], axis=1)
            sp = [state[p] for p in range(N_PAIRS)]
            for i in range(0, SUBLANES, 2):
                r0, r1 = slice(i, i + 1), slice(i + 1, i + 2)
                sums = [(_pair_dot(sp[p], a8_l[r0, pairs[p]], a8_r[r0, pairs[p]], left),
                         _pair_dot(sp[p], wa8_l[r0, pairs[p]], wa8_r[r0, pairs[p]], left)) for p in range(N_PAIRS)]
                sa0, sa1 = [s[0] for s in sums], [s[1] for s in sums]
                for p in range(N_PAIRS):
                    pr = pairs[p]
                    inner = slice((p % 2) * LANES, (p % 2 + 1) * LANES)
                    vt0 = vbuf[slot, p // 2, i * HEAD:(i + 1) * HEAD, inner]
                    vt1 = vbuf[slot, p // 2, (i + 1) * HEAD:(i + 2) * HEAD, inner]
                    sa_next = sa1[p] + sa0[p] * ba8[r0, pr] + vt0 * ka8[r0, pr]
                    s1 = sp[p] * w8[r0, pr] + sa0[p] * b8[r0, pr] + vt0 * k8[r0, pr]
                    st_ref[q * SUBLANES + i, p] = s1
                    _store_tile(qbuf, slot, p, i, s1 * r8[r0, pr])
                    s2 = s1 * w8[r1, pr] + sa_next * b8[r1, pr] + vt1 * k8[r1, pr]
                    st_ref[q * SUBLANES + i + 1, p] = s2
                    _store_tile(qbuf, slot, p, i + 1, s2 * r8[r1, pr])
                    sp[p] = s2
            for p in range(N_PAIRS):
                state[p] = sp[p]

        def y_rows(q, slot):
            for g2 in range(2):
                y_ref[rows_of(q), quads[g2]] = _diag_rows(qbuf[slot, g2], diag2, bd2, sub_row2)

        v_tiles(0, 0)

        def two_groups(j, carry):
            q0 = 2 * j
            v_tiles(q0 + 1, 1)
            chain(q0, 0)
            y_rows(jnp.maximum(q0 - 1, 0), 1)
            v_tiles(jnp.minimum(q0 + 2, groups - 1), 0)
            chain(q0 + 1, 1)
            y_rows(q0, 0)
            return carry

        lax.fori_loop(0, groups // 2, two_groups, 0)
        y_rows(groups - 1, 1)
        if moves:
            moves.wait(also=(g == nb - 1))

    rows = pl.BlockSpec((tb, D_HALF), lambda g: (g, 0))
    ex_in = exchange.operands if exchange else []
    ex_out = exchange.out_shapes if exchange else []
    res = pl.pallas_call(
        body, name="wkv_fwd", grid=(nb,),
        in_specs=[rows] * 6 + [ANY] * len(ex_in),
        out_specs=[rows, pl.BlockSpec((tb, N_PAIRS, HEAD, LANES), lambda g: (g, 0, 0, 0))] + [ANY] * len(ex_out),
        out_shape=[jax.ShapeDtypeStruct((s, D_HALF), F32),
                   jax.ShapeDtypeStruct((s, N_PAIRS, HEAD, LANES), F32)] + ex_out,
        scratch_shapes=[pltpu.VMEM((N_PAIRS, HEAD, LANES), F32),
                        pltpu.VMEM((2, 2, SUBLANES * HEAD, 2 * LANES), F32),
                        pltpu.VMEM((2, 2, SUBLANES * HEAD, 2 * LANES), BF16)]
                       + (exchange.scratch() if exchange else []),
        compiler_params=_params("arbitrary"),
    )(r, w, k, a, b, v, *ex_in)
    return res[0], res[1], list(res[2:])


def _wkv_bwd(r, w, k, a, b, v, dy, st, exchange=None):
    s = r.shape[0]
    tb = SCAN_TB
    nb = s // tb

    def body(*refs):
        ((r_ref, w_ref, k_ref, a_ref, b_ref, v_ref, dy_ref, st_ref, before_ref),
         (dr_ref, dw_ref, dk_ref, dv_ref, da_ref, db_ref), (dstate, vbuf, qbuf, sbuf),
         moves) = _split_refs(refs, 9, 6, exchange)
        g = pl.program_id(0)
        first_block = g == nb - 1
        if moves:
            moves.start(also=(g == 0))

        @pl.when(g == 0)
        def _():
            dstate[...] = jnp.zeros_like(dstate)
            qbuf[...] = jnp.zeros_like(qbuf)

        left = _left_half()
        diag2, bd2 = _quad_consts()
        sub_row = lax.broadcasted_iota(jnp.int32, (SUBLANES, LANES), 0)
        sub_row2 = lax.broadcasted_iota(jnp.int32, (SUBLANES, 2 * LANES), 0)
        groups = tb // SUBLANES
        quads = [slice(g2 * 2 * LANES, (g2 + 1) * 2 * LANES) for g2 in range(2)]
        row_refs = (dr_ref, dw_ref, dk_ref, da_ref, db_ref)

        def rows_of(q):
            return pl.ds(pl.multiple_of(q * SUBLANES, SUBLANES), SUBLANES)

        def state_before(q, i, p):
            if i > 0:
                return st_ref[q * SUBLANES + i - 1, p]
            return jnp.where(q == 0, jnp.where(first_block, 0.0, before_ref[0, p]),
                             st_ref[jnp.maximum(q * SUBLANES - 1, 0), p])

        def column_tiles(q, slot):
            rows8 = rows_of(q)
            for kind, ref in enumerate((v_ref, dy_ref)):
                x8 = ref[rows8, :]
                for g2 in range(2):
                    vbuf[slot, kind, g2] = _rows_to_columns(x8[:, quads[g2]], diag2, bd2)
            a8 = a_ref[rows8, :]
            for i in range(SUBLANES):
                for p in range(N_PAIRS):
                    _store_tile(sbuf, 0, p, i, state_before(q, i, p) * a8[i:i + 1, p * LANES:(p + 1) * LANES])
            for g2 in range(2):
                vbuf[slot, 2, g2] = jnp.dot(sbuf[0, g2], bd2, preferred_element_type=F32)

        def chain(q, slot):
            rows8 = rows_of(q)
            a8, w8, b8, k8, r8 = (x[rows8, :] for x in (a_ref, w_ref, b_ref, k_ref, r_ref))
            b8_l, b8_r = _halves(b8)
            dsp = [dstate[p] for p in range(N_PAIRS)]
            outs = [[jnp.zeros((SUBLANES, LANES), F32) for _ in row_refs] for _ in range(N_PAIRS)]
            after = [st_ref[q * SUBLANES + SUBLANES - 1, p] for p in range(N_PAIRS)]
            for i in reversed(range(SUBLANES)):
                row = slice(i, i + 1)
                pl_ = [slice(p * LANES, (p + 1) * LANES) for p in range(N_PAIRS)]
                tile = [(p // 2, slice(i * HEAD, (i + 1) * HEAD), slice((p % 2) * LANES, (p % 2 + 1) * LANES))
                        for p in range(N_PAIRS)]
                sp = [state_before(q, i, p) for p in range(N_PAIRS)]
                dyt = [vbuf[(slot, 1) + tile[p]] for p in range(N_PAIRS)]
                ds = [dsp[p] + dyt[p] * r8[row, pl_[p]] for p in range(N_PAIRS)]
                dsa = [_pair_dot(ds[p], b8_l[row, pl_[p]], b8_r[row, pl_[p]], left) for p in range(N_PAIRS)]
                sa = [vbuf[(slot, 2) + tile[p]] for p in range(N_PAIRS)]
                for p in range(N_PAIRS):
                    ar, wr, br, kr = (x[row, pl_[p]] for x in (a8, w8, b8, k8))
                    vt = vbuf[(slot, 0) + tile[p]]
                    dsp[p] = ds[p] * wr + dsa[p] * ar
                    new = (_colsum(after[p] * dyt[p]), _colsum(ds[p] * sp[p]), _colsum(ds[p] * vt),
                           _colsum(sp[p] * dsa[p]), _colsum(ds[p] * sa[p]))
                    outs[p] = [jnp.where(sub_row == i, n, o) for n, o in zip(new, outs[p])]
                    _store_tile(qbuf, slot, p, i, ds[p] * kr)
                after = sp
            for p in range(N_PAIRS):
                dstate[p] = dsp[p]
                for ref, o in zip(row_refs, outs[p]):
                    ref[rows8, p * LANES:(p + 1) * LANES] = o

        def dv_rows(q, slot):
            for g2 in range(2):
                dv_ref[rows_of(q), quads[g2]] = _diag_rows(qbuf[slot, g2], diag2, bd2, sub_row2)

        column_tiles(groups - 1, 0)

        def two_groups(j, carry):
            q0 = groups - 1 - 2 * j
            column_tiles(q0 - 1, 1)
            chain(q0, 0)
            dv_rows(jnp.minimum(q0 + 1, groups - 1), 1)
            column_tiles(jnp.maximum(q0 - 2, 0), 0)
            chain(q0 - 1, 1)
            dv_rows(q0, 0)
            return carry

        lax.fori_loop(0, groups // 2, two_groups, 0)
        dv_rows(0, 1)
        if moves:
            moves.wait(also=(g == nb - 1))

    rows = pl.BlockSpec((tb, D_HALF), lambda g: (nb - 1 - g, 0))
    ex_in = exchange.operands if exchange else []
    ex_out = exchange.out_shapes if exchange else []
    res = pl.pallas_call(
        body, name="wkv_bwd", grid=(nb,),
        in_specs=[rows] * 7 + [pl.BlockSpec((tb, N_PAIRS, HEAD, LANES), lambda g: (nb - 1 - g, 0, 0, 0)),
                               pl.BlockSpec((1, N_PAIRS, HEAD, LANES),
                                            lambda g: (jnp.maximum((nb - 1 - g) * tb - 1, 0), 0, 0, 0))]
                 + [ANY] * len(ex_in),
        out_specs=[rows] * 6 + [ANY] * len(ex_out),
        out_shape=[jax.ShapeDtypeStruct((s, D_HALF), F32)] * 6 + ex_out,
        scratch_shapes=[pltpu.VMEM((N_PAIRS, HEAD, LANES), F32),
                        pltpu.VMEM((2, 3, 2, SUBLANES * HEAD, 2 * LANES), F32),
                        pltpu.VMEM((2, 2, SUBLANES * HEAD, 2 * LANES), BF16),
                        pltpu.VMEM((1, 2, SUBLANES * HEAD, 2 * LANES), BF16)]
                       + (exchange.scratch() if exchange else []),
        compiler_params=_params("arbitrary"),
    )(r, w, k, a, b, v, dy, st, st, *ex_in)
    return list(res[:6]), list(res[6:])


def _rwkv_post_math(y, r, k, v, gate, lw, lb, rk, bd):
    mean = _head_sum(y, bd) * (1.0 / HEAD)
    yc = y - mean
    var = _head_sum(yc * yc, bd) * (1.0 / HEAD)
    rstd = lax.rsqrt(var + LNX_EPS)
    yn = yc * rstd
    rkk = _head_sum(r * k * rk, bd)
    sg = _sigmoid(gate)
    pre = yn * lw + lb + rkk * v
    return yn, rstd, rkk, sg, pre


def _rwkv_prep_bwd(u_a, h_t, grads, mu, wl, w0, a0, kkw, kaw, tm=256):
    s = u_a.shape[0]
    nb = s // tm
    d = h_t.shape[0]

    def body(ua_ref, prev_ref, ht_ref, drs_ref, dws_ref, dks_ref, dvs_ref, das_ref, dbs_ref, drb_ref, dkb_ref, dvb_ref,
             dgt_ref, mu_ref, wl_ref, w0_ref, a0_ref, kkw_ref, kaw_ref,
             du_ref, dwa_ref, dmu_ref, dwl_ref, dw0_ref, da0_ref, dkkw_ref, dkaw_ref, carry):
        i = pl.program_id(0)

        @pl.when(i == 0)
        def _():
            carry[...] = jnp.zeros_like(carry)
            for ref in (dwa_ref, dmu_ref, dwl_ref, dw0_ref, da0_ref, dkkw_ref, dkaw_ref):
                ref[...] = jnp.zeros_like(ref)

        bd = _head_ones()
        mu_v, wl_v, kkw_v, kaw_v = mu_ref[...], wl_ref[...], kkw_ref[...], kaw_ref[...]
        f = _rwkv_elementwise(ua_ref[...], prev_ref[7:8, :], i == nb - 1, mu_v, wl_v, w0_ref[...],
                              a0_ref[...], kkw_v, kaw_v, bd)
        a, kk, k0 = f["a"], f["kk"], f["k0"]
        dk = dks_ref[...] + dkb_ref[...]
        dbs = dbs_ref[...]
        dkk = dbs * a - das_ref[...]
        da = dbs * kk + dk * k0 * kaw_v
        dk0 = dk * (1.0 + (a - 1.0) * kaw_v)
        dkaw_ref[...] += _colsum(dk * k0 * (a - 1.0))
        inv = 1.0 / f["nrm"]
        proj = _head_sum(dkk * kk, bd)
        dkk0 = jnp.where(f["ss"] > 1e-24, (dkk - kk * proj) * inv, dkk * inv)
        dk0 = dk0 + dkk0 * kkw_v
        dkkw_ref[...] += _colsum(dkk0 * k0)
        dza = da * a * (1.0 - a)
        da0_ref[...] += _colsum(dza)
        dz = -dws_ref[...] * f["dec"] * f["e"] * (1.0 - f["sz"])
        dw0_ref[...] += _colsum(dz)
        dll = jnp.concatenate([dz, dza], axis=1).astype(BF16)
        dwl_ref[...] += _dot_tn(f["lin"].astype(BF16), dll)
        dlin = _dot_nt(dll, wl_v)
        lane = lax.broadcasted_iota(jnp.int32, (1, LANES), 1)
        th = f["th"]
        dlo = jnp.where(lane < LORA, dlin * (1.0 - th * th), dlin)
        dus = jnp.concatenate([drs_ref[...] + drb_ref[...], dk0, dvs_ref[...] + dvb_ref[...], dlo, dgt_ref[...]],
                              axis=1)
        dmu_ref[...] += _colsum(dus * f["delta"])
        g1 = dus * mu_v
        rows = lax.broadcasted_iota(jnp.int32, (tm, 1), 0)
        up = jnp.where(rows == tm - 1, carry[...], pltpu.roll(g1, tm - 1, 0))
        dua = dus - g1 + up
        du_ref[...] = dua
        dwa_ref[...] += jnp.dot(ht_ref[...], dua.astype(BF16), preferred_element_type=F32)
        carry[...] = g1[0:1, :]

    rev = lambda w: pl.BlockSpec((tm, w), lambda i: (nb - 1 - i, 0))
    vec = lambda w: pl.BlockSpec((1, w), lambda i: (0, 0))
    wl_spec = pl.BlockSpec((LANES, 2 * D_HALF), lambda i: (0, 0))
    return pl.pallas_call(
        body, name="rwkv_prep_bwd", grid=(nb,),
        in_specs=[rev(SEC), pl.BlockSpec((8, SEC), lambda i: (jnp.maximum((nb - 1 - i) * (tm // 8) - 1, 0), 0)),
                  pl.BlockSpec((d, tm), lambda i: (0, nb - 1 - i))]
                 + [rev(D_HALF)] * 10 + [vec(SEC), wl_spec] + [vec(D_HALF)] * 4,
        out_specs=[rev(SEC), pl.BlockSpec((d, SEC), lambda i: (0, 0)), vec(SEC), wl_spec] + [vec(D_HALF)] * 4,
        out_shape=[jax.ShapeDtypeStruct((s, SEC), F32), jax.ShapeDtypeStruct((d, SEC), F32),
                   jax.ShapeDtypeStruct((1, SEC), F32),
                   jax.ShapeDtypeStruct((LANES, 2 * D_HALF), F32)] + [jax.ShapeDtypeStruct((1, D_HALF), F32)] * 4,
        scratch_shapes=[pltpu.VMEM((1, SEC), F32)],
        compiler_params=_params("arbitrary"),
    )(u_a, u_a, h_t, *grads, mu, wl, w0, a0, kkw, kaw)


def _tri(tm, lower):
    r = lax.broadcasted_iota(jnp.int32, (tm, tm), 0)
    c = lax.broadcasted_iota(jnp.int32, (tm, tm), 1)
    return ((r >= c) if lower else (r <= c)).astype(BF16)


def _head_rms(x, g, bd):
    rinv = lax.rsqrt(_head_sum(x * x, bd) * (1.0 / HEAD) + RMS_EPS)
    xh = x * rinv
    return xh, rinv, xh * g


def _fox_front(h, w_b, fb, qg, kg, tm=256):
    s, d = h.shape

    def body(h_ref, wb_ref, fb_ref, qg_ref, kg_ref, ub_ref, q_ref, k_ref, v_ref, cc_ref, cr_ref, carry):
        i = pl.program_id(0)

        @pl.when(i == 0)
        def _():
            carry[...] = jnp.zeros_like(carry)

        ub_ref[...] = jnp.dot(h_ref[...], wb_ref[...], preferred_element_type=F32)
        bd = _head_ones()
        _, _, qn = _head_rms(ub_ref[:, 0:512], qg_ref[...], bd)
        _, _, kn = _head_rms(ub_ref[:, 512:1024], kg_ref[...], bd)
        q_ref[...] = (qn * ATT_SCALE).astype(BF16)
        k_ref[...] = kn.astype(BF16)
        v_ref[...] = ub_ref[:, 1024:1536].astype(BF16)
        lane = lax.broadcasted_iota(jnp.int32, (1, LANES), 1)
        logf = jnp.where(lane < N_HEADS, _log_sigmoid(ub_ref[:, 2048:2176] + fb_ref[...]), 0.0)
        cum = _exact_dot(logf, _tri(tm, True), ones_first=True) + carry[...]
        for h in range(N_HEADS):
            cc_ref[h] = jnp.broadcast_to(cum[:, h:h + 1], (tm, LANES))
        cr_ref[...] = jnp.transpose(cum)[0:N_HEADS, :]
        carry[...] = cum[tm - 1:tm, :]

    blk = pl.BlockSpec((tm, D_HALF), lambda i: (i, 0))
    return pl.pallas_call(
        body, name="fox_front", grid=(s // tm,),
        in_specs=[pl.BlockSpec((tm, d), lambda i: (i, 0)),
                  pl.BlockSpec(w_b.shape, lambda i: (0, 0), pipeline_mode=pl.Buffered(1)),
                  pl.BlockSpec((1, LANES), lambda i: (0, 0)),
                  pl.BlockSpec((1, D_HALF), lambda i: (0, 0)), pl.BlockSpec((1, D_HALF), lambda i: (0, 0))],
        out_specs=[pl.BlockSpec((tm, SEC), lambda i: (i, 0)), blk, blk, blk,
                   pl.BlockSpec((N_HEADS, tm, LANES), lambda i: (0, i, 0)), pl.BlockSpec((N_HEADS, tm), lambda i: (0, i))],
        out_shape=[jax.ShapeDtypeStruct((s, SEC), F32)] + [jax.ShapeDtypeStruct((s, D_HALF), BF16)] * 3
                  + [jax.ShapeDtypeStruct((N_HEADS, s, LANES), F32), jax.ShapeDtypeStruct((N_HEADS, s), F32)],
        scratch_shapes=[pltpu.VMEM((1, LANES), F32)],
        compiler_params=_params("arbitrary"),
    )(h, w_b, fb, qg, kg)


ATT_T = 256


def _tiles(nblk, by_query):
    if by_query:
        pairs = [(i, j) for i in range(nblk) for j in range(i + 1)]
    else:
        pairs = [(i, j) for j in range(nblk) for i in range(j, nblk)]
    return (jnp.asarray([p[0] for p in pairs], jnp.int32), jnp.asarray([p[1] for p in pairs], jnp.int32))


def _attn_fwd(q, k, v, cc, cr):
    s = q.shape[0]
    t = ATT_T
    nblk = s // t

    def body(qi_ref, kj_ref, q_ref, k_ref, v_ref, cc_ref, cr_ref, o_ref, lse_ref, m_sc, l_sc, acc_sc):
        i = qi_ref[pl.program_id(0)]
        j = kj_ref[pl.program_id(0)]

        @pl.when(j == 0)
        def _():
            m_sc[...] = jnp.full_like(m_sc, NEG)
            l_sc[...] = jnp.zeros_like(l_sc)
            acc_sc[...] = jnp.zeros_like(acc_sc)

        def tile(on_diagonal):
            causal = _causal_tile(t) if on_diagonal else None
            left = lax.broadcasted_iota(jnp.int32, (1, LANES), 1) < HEAD
            for p in range(N_PAIRS):
                lanes = slice(p * LANES, (p + 1) * LANES)
                q2, k2, v2 = q_ref[:, lanes], k_ref[:, lanes], v_ref[:, lanes]
                acc2 = acc_sc[:, lanes]
                for e in range(2):
                    h = 2 * p + e
                    msk = left if e == 0 else jnp.logical_not(left)
                    sc = _dot_nt(jnp.where(msk, q2, jnp.zeros_like(q2)), k2)
                    sc = sc + (_wide(cc_ref[h]) - cr_ref[h:h + 1, :])
                    if on_diagonal:
                        sc = jnp.where(causal, sc, NEG)
                    m_prev = m_sc[h]
                    m_new = jnp.maximum(m_prev, jnp.max(sc, axis=1, keepdims=True))
                    alpha = jnp.exp(m_prev - m_new)
                    pm = jnp.exp(sc - _wide(m_new))
                    l_sc[h] = alpha * l_sc[h] + jnp.sum(pm, axis=1, keepdims=True)
                    m_sc[h] = m_new
                    pv = jnp.dot(pm.astype(BF16), v2, preferred_element_type=F32)
                    acc2 = jnp.where(msk, alpha * acc2 + pv, acc2)
                acc_sc[:, lanes] = acc2

        pl.when(j < i)(functools.partial(tile, False))
        pl.when(j == i)(functools.partial(tile, True))

        @pl.when(j == i)
        def _():
            left = lax.broadcasted_iota(jnp.int32, (1, LANES), 1) < HEAD
            for p in range(N_PAIRS):
                lanes = slice(p * LANES, (p + 1) * LANES)
                inv = jnp.where(left, 1.0 / l_sc[2 * p], 1.0 / l_sc[2 * p + 1])
                o_ref[:, lanes] = acc_sc[:, lanes] * inv
            for h in range(N_HEADS):
                lse_ref[h] = m_sc[h] + jnp.log(l_sc[h])

    qi, kj = _tiles(nblk, by_query=True)
    qblk = pl.BlockSpec((t, D_HALF), lambda n, qi, kj: (qi[n], 0))
    kblk = pl.BlockSpec((t, D_HALF), lambda n, qi, kj: (kj[n], 0))
    qrep = pl.BlockSpec((N_HEADS, t, LANES), lambda n, qi, kj: (0, qi[n], 0))
    return pl.pallas_call(
        body, name="fox_attn_fwd",
        grid_spec=pltpu.PrefetchScalarGridSpec(
            num_scalar_prefetch=2, grid=(qi.shape[0],),
            in_specs=[qblk, kblk, kblk, qrep, pl.BlockSpec((N_HEADS, t), lambda n, qi, kj: (0, kj[n]))],
            out_specs=[qblk, qrep],
            scratch_shapes=[pltpu.VMEM((N_HEADS, t, LANES), F32), pltpu.VMEM((N_HEADS, t, LANES), F32),
                            pltpu.VMEM((t, D_HALF), F32)]),
        out_shape=[jax.ShapeDtypeStruct((s, D_HALF), F32), jax.ShapeDtypeStruct((N_HEADS, s, LANES), F32)],
        compiler_params=_params("arbitrary"),
    )(qi, kj, q, k, v, cc, cr)


def _causal_tile(t):
    return lax.broadcasted_iota(jnp.int32, (t, t), 0) >= lax.broadcasted_iota(jnp.int32, (t, t), 1)


def _wide(x):
    return jnp.concatenate([x, x], axis=1)


def _attn_probs(q2, k2, v2, do2, msk, causal, bias, lse_rows):
    zero = jnp.zeros_like(q2)
    qh = jnp.where(msk, q2, zero)
    doh = jnp.where(msk, do2, zero)
    sc = _dot_nt(qh, k2) + bias
    if causal is not None:
        sc = jnp.where(causal, sc, NEG)
    pm = jnp.exp(sc - _wide(lse_rows))
    dp = _dot_nt(doh, v2)
    return qh, doh, pm, dp


def _attn_bwd_rowdot(q, k, v, do, lse, cc, cr):
    s = q.shape[0]
    t = ATT_T
    nblk = s // t

    def body(qi_ref, kj_ref, q_ref, k_ref, v_ref, do_ref, lse_ref, cc_ref, cr_ref, dd_ref, acc):
        i = qi_ref[pl.program_id(0)]
        j = kj_ref[pl.program_id(0)]

        @pl.when(j == 0)
        def _():
            acc[...] = jnp.zeros_like(acc)

        def tile(on_diagonal):
            causal = _causal_tile(t) if on_diagonal else None
            left = lax.broadcasted_iota(jnp.int32, (1, LANES), 1) < HEAD
            for p in range(N_PAIRS):
                lanes = slice(p * LANES, (p + 1) * LANES)
                q2, k2, v2, do2 = q_ref[:, lanes], k_ref[:, lanes], v_ref[:, lanes], do_ref[:, lanes]
                for e in range(2):
                    h = 2 * p + e
                    msk = left if e == 0 else jnp.logical_not(left)
                    bias = _wide(cc_ref[h]) - cr_ref[h:h + 1, :]
                    _, _, pm, dp = _attn_probs(q2, k2, v2, do2, msk, causal, bias, lse_ref[h])
                    acc[h] += jnp.sum(pm * dp, axis=1, keepdims=True)

        pl.when(j < i)(functools.partial(tile, False))
        pl.when(j == i)(functools.partial(tile, True))

        @pl.when(j == i)
        def _():
            dd_ref[...] = acc[...]

    qi, kj = _tiles(nblk, by_query=True)
    qblk = pl.BlockSpec((t, D_HALF), lambda n, qi, kj: (qi[n], 0))
    qcol = pl.BlockSpec((N_HEADS, t, LANES), lambda n, qi, kj: (0, qi[n], 0))
    kblk = pl.BlockSpec((t, D_HALF), lambda n, qi, kj: (kj[n], 0))
    return pl.pallas_call(
        body, name="fox_attn_rowdot",
        grid_spec=pltpu.PrefetchScalarGridSpec(
            num_scalar_prefetch=2, grid=(qi.shape[0],),
            in_specs=[qblk, kblk, kblk, qblk, qcol, qcol, pl.BlockSpec((N_HEADS, t), lambda n, qi, kj: (0, kj[n]))],
            out_specs=qcol, scratch_shapes=[pltpu.VMEM((N_HEADS, t, LANES), F32)]),
        out_shape=jax.ShapeDtypeStruct((N_HEADS, s, LANES), F32),
        compiler_params=_params("arbitrary"),
    )(qi, kj, q, k, v, do, lse, cc, cr)


def _attn_bwd(q, k, v, do, lse, dd, cc, cr):
    s = q.shape[0]
    t = ATT_T
    nblk = s // t

    def body(qi_ref, kj_ref, q_ref, k_ref, v_ref, do_ref, lse_ref, dd_ref, cc_ref, cr_ref,
             dq_ref, dk_ref, dv_ref, dcr_ref, dk_sc, dv_sc, dcr_sc):
        i = qi_ref[pl.program_id(0)]
        j = kj_ref[pl.program_id(0)]

        @pl.when(pl.program_id(0) == 0)
        def _():
            dq_ref[...] = jnp.zeros_like(dq_ref)

        @pl.when(i == j)
        def _():
            dk_sc[...] = jnp.zeros_like(dk_sc)
            dv_sc[...] = jnp.zeros_like(dv_sc)
            dcr_sc[...] = jnp.zeros_like(dcr_sc)

        def tile(on_diagonal):
            causal = _causal_tile(t) if on_diagonal else None
            left = lax.broadcasted_iota(jnp.int32, (1, LANES), 1) < HEAD
            qrows = pl.ds(pl.multiple_of(i * t, t), t)
            for p in range(N_PAIRS):
                lanes = slice(p * LANES, (p + 1) * LANES)
                q2, k2, v2, do2 = q_ref[:, lanes], k_ref[:, lanes], v_ref[:, lanes], do_ref[:, lanes]
                zero = jnp.zeros_like(q2)
                dq2 = jnp.zeros((t, LANES), F32)
                dk2 = jnp.zeros((t, LANES), F32)
                dv2 = jnp.zeros((t, LANES), F32)
                for e in range(2):
                    h = 2 * p + e
                    msk = left if e == 0 else jnp.logical_not(left)
                    bias = _wide(cc_ref[h]) - cr_ref[h:h + 1, :]
                    qh, doh, pm, dp = _attn_probs(q2, k2, v2, do2, msk, causal, bias, lse_ref[h])
                    dsc = pm * (dp - _wide(dd_ref[h]))
                    dsb = dsc.astype(BF16)
                    dv2 += _dot_tn(pm.astype(BF16), doh)
                    dk2 += _dot_tn(dsb, qh)
                    dq2 += jnp.dot(dsb, jnp.where(msk, k2, zero), preferred_element_type=F32)
                    dcr_sc[h:h + 1, :] += -_colsum(dsc)
                dq_ref[qrows, lanes] += dq2 * ATT_SCALE
                dk_sc[:, lanes] += dk2
                dv_sc[:, lanes] += dv2

        pl.when(i > j)(functools.partial(tile, False))
        pl.when(i == j)(functools.partial(tile, True))

        @pl.when(i == nblk - 1)
        def _():
            dk_ref[...] = dk_sc[...]
            dv_ref[...] = dv_sc[...]
            dcr_ref[...] = dcr_sc[...]

    qi, kj = _tiles(nblk, by_query=False)
    qblk = pl.BlockSpec((t, D_HALF), lambda n, qi, kj: (qi[n], 0))
    qcol = pl.BlockSpec((N_HEADS, t, LANES), lambda n, qi, kj: (0, qi[n], 0))
    kblk = pl.BlockSpec((t, D_HALF), lambda n, qi, kj: (kj[n], 0))
    krow = pl.BlockSpec((N_HEADS, t), lambda n, qi, kj: (0, kj[n]))
    return pl.pallas_call(
        body, name="fox_attn_bwd",
        grid_spec=pltpu.PrefetchScalarGridSpec(
            num_scalar_prefetch=2, grid=(qi.shape[0],),
            in_specs=[qblk, kblk, kblk, qblk, qcol, qcol, qcol, krow],
            out_specs=[pl.BlockSpec((s, D_HALF), lambda n, qi, kj: (0, 0)), kblk, kblk, krow],
            scratch_shapes=[pltpu.VMEM((t, D_HALF), F32), pltpu.VMEM((t, D_HALF), F32), pltpu.VMEM((N_HEADS, t), F32)]),
        out_shape=[jax.ShapeDtypeStruct((s, D_HALF), F32)] * 3 + [jax.ShapeDtypeStruct((N_HEADS, s), F32)],
        compiler_params=_params("arbitrary"),
    )(qi, kj, q, k, v, do, lse, dd, cc, cr)


def _fox_prep_bwd(u_b, h_t, dq, dk, dv, dgate, dcum, fb, qg, kg, tm=256):
    s = u_b.shape[0]
    nb = s // tm
    d = h_t.shape[0]

    def body(ub_ref, ht_ref, dq_ref, dk_ref, dv_ref, dg_ref, dc_ref, fb_ref, qg_ref, kg_ref,
             du_ref, dwb_ref, dqg_ref, dkg_ref, dfb_ref, carry):
        i = pl.program_id(0)

        @pl.when(i == 0)
        def _():
            carry[...] = jnp.zeros_like(carry)
            dwb_ref[...] = jnp.zeros_like(dwb_ref)
            dqg_ref[...] = jnp.zeros_like(dqg_ref)
            dkg_ref[...] = jnp.zeros_like(dkg_ref)
            dfb_ref[...] = jnp.zeros_like(dfb_ref)

        bd = _head_ones()
        for lo, g_ref, d_ref, dgain_ref in ((0, qg_ref, dq_ref, dqg_ref), (512, kg_ref, dk_ref, dkg_ref)):
            gain = g_ref[...]
            xh, rinv, _ = _head_rms(ub_ref[:, lo:lo + 512], gain, bd)
            dn = d_ref[...]
            dgain_ref[...] += _colsum(dn * xh)
            dxh = dn * gain
            du_ref[:, lo:lo + 512] = rinv * (dxh - xh * (_head_sum(dxh * xh, bd) * (1.0 / HEAD)))
        du_ref[:, 1024:1536] = dv_ref[...]
        du_ref[:, 1536:2048] = dg_ref[...]
        lane = lax.broadcasted_iota(jnp.int32, (1, LANES), 1)
        dc = dc_ref[...]
        dlogf = _exact_dot(dc, _tri(tm, False), ones_first=True) + carry[...]
        carry[...] += _colsum(dc)
        fl = ub_ref[:, 2048:2176] + fb_ref[...]
        dfl = jnp.where(lane < N_HEADS, dlogf * (1.0 - _sigmoid(fl)), 0.0)
        du_ref[:, 2048:2176] = dfl
        dfb_ref[...] += _colsum(dfl)
        dwb_ref[...] += jnp.dot(ht_ref[...], du_ref[...].astype(BF16), preferred_element_type=F32)

    rev = lambda w: pl.BlockSpec((tm, w), lambda i: (nb - 1 - i, 0))
    vec = lambda w: pl.BlockSpec((1, w), lambda i: (0, 0))
    return pl.pallas_call(
        body, name="fox_prep_bwd", grid=(nb,),
        in_specs=[rev(SEC), pl.BlockSpec((d, tm), lambda i: (0, nb - 1 - i))] + [rev(D_HALF)] * 4
                 + [rev(LANES), vec(LANES), vec(D_HALF), vec(D_HALF)],
        out_specs=[rev(SEC), pl.BlockSpec((d, SEC), lambda i: (0, 0)), vec(D_HALF), vec(D_HALF), vec(LANES)],
        out_shape=[jax.ShapeDtypeStruct((s, SEC), F32), jax.ShapeDtypeStruct((d, SEC), F32),
                   jax.ShapeDtypeStruct((1, D_HALF), F32), jax.ShapeDtypeStruct((1, D_HALF), F32),
                   jax.ShapeDtypeStruct((1, LANES), F32)],
        scratch_shapes=[pltpu.VMEM((1, LANES), F32)],
        compiler_params=_params("arbitrary"),
    )(u_b, h_t, dq, dk, dv, dgate, dcum, fb, qg, kg)


def _merge(y, r, k, v, gate_a, o, u_b, h, x, tgt, w_g, wa, wb, wo, fg, lw, lb, rk, tm=256):
    s, d = x.shape

    def body(y_ref, r_ref, k_ref, v_ref, ga_ref, o_ref, gb_ref, h_ref, x_ref, t_ref, wg_ref, wa_ref, wb_ref, wo_ref,
             fg_ref, lw_ref, lb_ref, rk_ref,
             dx2_ref, dy_ref, drb_ref, dkb_ref, dvb_ref, dga_ref, do_ref, dgb_ref, dug_ref,
             dwa_ref, dwb_ref, dwo_ref, dfg_ref, loss_ref, dlw_ref, dlb_ref, drk_ref):
        i = pl.program_id(0)

        @pl.when(i == 0)
        def _():
            for ref in (dwa_ref, dwb_ref, dwo_ref, dfg_ref, loss_ref, dlw_ref, dlb_ref, drk_ref):
                ref[...] = jnp.zeros_like(ref)

        bd = _head_ones()
        wa_v, wb_v, wo_v, fg_v = wa_ref[...], wb_ref[...], wo_ref[...], fg_ref[...]
        rv, kv, vv, ga, lw_v, rk_v = r_ref[...], k_ref[...], v_ref[...], ga_ref[...], lw_ref[...], rk_ref[...]
        yn, rstd, rkk, sga, pre = _rwkv_post_math(y_ref[...], rv, kv, vv, ga, lw_v, lb_ref[...], rk_v, bd)
        silu_a = ga * sga
        gb, ov = gb_ref[...], o_ref[...]
        sgb = _sigmoid(gb)
        silu_b = gb * sgb
        ma = (pre * silu_a).astype(BF16)
        mb = (ov * silu_b).astype(BF16)
        ya = jnp.dot(ma, wa_v, preferred_element_type=F32)
        yb = jnp.dot(mb, wb_v, preferred_element_type=F32)
        ug = jnp.dot(h_ref[...], wg_ref[...], preferred_element_type=F32)
        sa = _sigmoid(ug[:, 0:d])
        sb = _sigmoid(ug[:, d:2 * d])
        merged = (sa * ya + sb * yb).astype(BF16)
        x2 = x_ref[...] + jnp.dot(merged, wo_v, preferred_element_type=F32)
        r2 = lax.rsqrt(jnp.mean(x2 * x2, axis=-1, keepdims=True) + RMS_EPS)
        x2h = x2 * r2
        err = x2h * fg_v - t_ref[...]
        loss_ref[...] += _colsum(err * err)
        dyo = err * (1.0 / d)
        dfg_ref[...] += _colsum(dyo * x2h)
        dx2h = dyo * fg_v
        dx2 = r2 * (dx2h - x2h * jnp.mean(dx2h * x2h, axis=-1, keepdims=True))
        dx2_ref[...] = dx2
        dx2b = dx2.astype(BF16)
        dmerged = _dot_nt(dx2b, wo_v)
        dwo_ref[...] += _dot_tn(merged, dx2b)
        dya = dmerged * sa
        dyb = dmerged * sb
        dug_ref[:, 0:d] = dya * ya * (1.0 - sa)
        dug_ref[:, d:2 * d] = dyb * yb * (1.0 - sb)
        dyab = dya.astype(BF16)
        dybb = dyb.astype(BF16)
        dwa_ref[...] += _dot_tn(ma, dyab)
        dwb_ref[...] += _dot_tn(mb, dybb)
        dmb = _dot_nt(dybb, wb_v)
        do_ref[...] = (dmb * silu_b).astype(BF16)
        dgb_ref[...] = dmb * ov * (sgb * (1.0 + gb * (1.0 - sgb)))
        dma = _dot_nt(dyab, wa_v)
        dga_ref[...] = dma * pre * (sga * (1.0 + ga * (1.0 - sga)))
        dpre = dma * silu_a
        dlw_ref[...] += _colsum(dpre * yn)
        dlb_ref[...] += _colsum(dpre)
        dyn = dpre * lw_v
        m1 = _head_sum(dyn, bd) * (1.0 / HEAD)
        m2 = _head_sum(dyn * yn, bd) * (1.0 / HEAD)
        dy_ref[...] = rstd * (dyn - m1 - yn * m2)
        dvb_ref[...] = dpre * rkk
        drkk = _head_sum(dpre * vv, bd)
        drb_ref[...] = drkk * kv * rk_v
        dkb_ref[...] = drkk * rv * rk_v
        drk_ref[...] += _colsum(drkk * rv * kv)

    row = lambda w: pl.BlockSpec((tm, w), lambda i: (i, 0))
    full = lambda a: pl.BlockSpec(a.shape, lambda i: (0, 0))
    once = lambda a: pl.BlockSpec(a.shape, lambda i: (0, 0), pipeline_mode=pl.Buffered(1))
    half = jax.ShapeDtypeStruct((s, D_HALF), F32)
    fshape = lambda a: jax.ShapeDtypeStruct(a.shape, F32)
    return pl.pallas_call(
        body, name="merge_fwd_bwd", grid=(s // tm,),
        in_specs=[row(D_HALF)] * 6 + [pl.BlockSpec((tm, D_HALF), lambda i: (i, 3)), row(d), row(d), row(d),
                                      once(w_g), once(wa), once(wb), once(wo), full(fg), full(lw), full(lb), full(rk)],
        out_specs=[row(d)] + [row(D_HALF)] * 7 + [row(GATE_COLS), full(wa), full(wb), full(wo), full(fg), full(fg),
                                                   full(lw), full(lb), full(rk)],
        out_shape=[jax.ShapeDtypeStruct((s, d), F32)] + [half] * 5 + [jax.ShapeDtypeStruct((s, D_HALF), BF16), half,
                                                                    jax.ShapeDtypeStruct((s, GATE_COLS), F32),
                                                                    fshape(wa), fshape(wb), fshape(wo), fshape(fg),
                                                                    fshape(fg), fshape(lw), fshape(lb), fshape(rk)],
        compiler_params=_params("arbitrary"),
    )(y, r, k, v, gate_a, o, u_b, h, x, tgt, w_g, wa, wb, wo, fg, lw, lb, rk)


def _lora_weight(w_up, a_up):
    z = jnp.zeros((LORA, D_HALF), w_up.dtype)
    return jnp.concatenate([jnp.concatenate([w_up, z], axis=1), jnp.concatenate([z, a_up], axis=1)], axis=0)


def _device_grads(x, tgt, p, w_a, w_up, a_up, late_weights, fwd_exchange=None, bwd_exchange=None, tail_exchange=None):
    wl = _lora_weight(w_up, a_up)
    rk = p["r_k"].reshape(1, D_HALF)
    fb = jnp.pad(p["f_bias"], ((0, 0), (0, LANES - N_HEADS)))
    qg = jnp.tile(p["q_norm_g"], (1, N_HEADS))
    kg = jnp.tile(p["k_norm_g"], (1, N_HEADS))
    fg = p["final_norm_g"].reshape(1, D_MODEL)
    mixer = (p["shift_mu"], wl, p["w0"], p["a0"], p["k_k"], p["k_a"])

    h, u_a, r, dec, k, v, av, bv, gate_a = _rwkv_front(x, p["norm_g"], w_a, *mixer)
    y, st, arrived = _wkv_fwd(r, dec, k, av, bv, v, fwd_exchange)

    w_b, w_g, w_out_a, w_out_b, w_out = late_weights(arrived)
    u_b, q, kn, vb, cc, cr = _fox_front(h, w_b, fb, qg, kg)
    o, lse = _attn_fwd(q, kn, vb, cc, cr)

    (dx2, dy, dr_b, dk_b, dv_b, dgate_a, do, dgate_b, du_g, dwa, dwb, dwo, dfg, loss_vec, dlw, dlb, drk) = _merge(
        y, r, k, v, gate_a, o, u_b, h, x, tgt, w_g, w_out_a, w_out_b, w_out, fg, p["lnx_w"], p["lnx_b"], rk)

    dd = _attn_bwd_rowdot(q, kn, vb, do, lse, cc, cr)
    dq, dk_att, dv_att, dcr = _attn_bwd(q, kn, vb, do, lse, dd, cc, cr)
    dcum = jnp.pad(dcr.T, ((0, 0), (0, LANES - N_HEADS)))
    h_t = h.T
    du_b, dw_b, dqg, dkg, dfb = _fox_prep_bwd(u_b, h_t, dq, dk_att, dv_att, dgate_b, dcum, fb, qg, kg)
    dw_g = _matmul_tn_acc(h_t, du_g, "dw_gate")

    scan_grads, sent = _wkv_bwd(r, dec, k, av, bv, v, dy, st,
                                bwd_exchange(dw_b, dw_g, dwa, dwb, dwo) if bwd_exchange else None)
    du_a, dw_a, dmu, dwl, dw0, da0, dkkw, dkaw = _rwkv_prep_bwd(
        u_a, h_t, (*scan_grads, dr_b, dk_b, dv_b, dgate_a), *mixer)
    dw_up, da_up = dwl[:LORA, :D_HALF], dwl[LORA:, D_HALF:]
    sent_last = _run_on_sequencer(tail_exchange(dw_a, dw_up, da_up), "scatter_tail", 1) if tail_exchange else []
    grad_x, dnorm_g, _ = _inproj_bwd(du_a, du_b, du_g, w_a, w_b, w_g, x, dx2, p["norm_g"])

    grads = dict(
        norm_g=dnorm_g, w_in=(dw_a, dw_b, dw_g), shift_mu=dmu,
        w_lora_up=dw_up, w0=dw0, a_lora_up=da_up, a0=da0, k_k=dkkw, k_a=dkaw,
        r_k=drk.reshape(1, N_HEADS, HEAD), lnx_w=dlw, lnx_b=dlb, f_bias=dfb[:, :N_HEADS],
        q_norm_g=dqg.reshape(N_HEADS, HEAD).sum(axis=0, keepdims=True),
        k_norm_g=dkg.reshape(N_HEADS, HEAD).sum(axis=0, keepdims=True),
        w_out_a=dwa, w_out_b=dwb, w_out=dwo, final_norm_g=dfg.reshape(D_MODEL))
    return loss_vec, grad_x, grads, sent, sent_last


CHIP_FLIPS = ((1, 0), (0, 1), (1, 1))
ANY = pl.BlockSpec(memory_space=pl.ANY)


def _position():
    return lax.axis_index("x"), lax.axis_index("y"), lax.axis_index("c")


def _flip(v, f):
    return 1 - v if f else v


def _both(a, b):
    if a is None:
        return b
    return a if b is None else jnp.logical_and(a, b)


def _when(cond, fn):
    if cond is None:
        fn()
    else:
        pl.when(cond)(fn)


class _Moves:
    def __init__(self, send_sems, recv_sems, local_sems):
        self.send_sems, self.recv_sems, self.local_sems = send_sems, recv_sems, local_sems
        self.remote, self.local = [], []

    def send(self, src, dst, peer, landing, send_if=None, recv_if=None, first=False):
        k = len(self.remote)
        sems = dict(send_sem=self.send_sems.at[k], recv_sem=self.recv_sems.at[k], device_id=peer, device_id_type=MESH)
        out = pltpu.make_async_remote_copy(src_ref=src, dst_ref=dst, **sems)
        arrival = pltpu.make_async_remote_copy(src_ref=src, dst_ref=landing, **sems)
        self.remote.append((out, arrival, send_if, recv_if, first))

    def copy(self, src, dst, cond=None):
        cp = pltpu.make_async_copy(src, dst, self.local_sems.at[len(self.local)])
        self.local.append((cp, cond))

    def start(self, also=None):
        for cp, cond in self.local:
            _when(_both(also, cond), cp.start)
        for out, _, send_if, _, _ in self.remote:
            _when(_both(also, send_if), out.start)

    def wait_arrivals(self, also=None, first=None):
        for _, arrival, _, recv_if, is_first in self.remote:
            if first is None or first == is_first:
                _when(_both(also, recv_if), arrival.wait_recv)

    def wait_sent(self, also=None):
        for out, _, send_if, _, _ in self.remote:
            _when(_both(also, send_if), out.wait_send)
        for cp, cond in self.local:
            _when(_both(also, cond), cp.wait)

    def wait(self, also=None):
        self.wait_arrivals(also)
        self.wait_sent(also)


class _Exchange:
    def __init__(self, operands, out_shapes, n_remote, n_local, build, relays=None, in_place=(), n_staging=0):
        self.operands, self.out_shapes = list(operands), list(out_shapes)
        self.n_remote, self.n_local, self.build = n_remote, n_local, build
        self.relays, self.in_place = relays, in_place
        self.n_staging = n_staging

    def scratch(self):
        return [pltpu.SemaphoreType.DMA((self.n_remote,)), pltpu.SemaphoreType.DMA((self.n_remote,)),
                pltpu.SemaphoreType.DMA((max(self.n_local, 1),))]

    def moves(self, in_refs, out_refs, sems):
        mv = _Moves(*sems)
        self.build(mv, in_refs, out_refs)
        return mv


def _run_on_sequencer(exchange, name, collective_id):
    ins = [jax.new_ref(a, memory_space=pltpu.MemorySpace.HBM) for a in exchange.operands]
    outs = [ins[i] if i in exchange.in_place else jax.empty_ref(s, memory_space=pltpu.MemorySpace.HBM)
            for i, s in enumerate(exchange.out_shapes)]
    forward, to_sibling = exchange.relays or (None, None)
    relay_scratch = [pltpu.SemaphoreType.DMA((stage[0],)) for stage in (forward, to_sibling) if stage for _ in range(2)]

    def launch(*sems):
        x, y, c = _position()
        peers = [(_flip(x, fx), _flip(y, fy), c) for fx, fy in CHIP_FLIPS] + ([(x, y, 1 - c)] if to_sibling else [])
        barrier = pltpu.get_barrier_semaphore()
        for peer in peers:
            pl.semaphore_signal(barrier, inc=1, device_id=peer, device_id_type=MESH)
        pl.semaphore_wait(barrier, len(peers))
        moves = exchange.moves(ins, outs, sems[:3])
        moves.start()
        later = []
        if forward:
            onward = _Moves(sems[3], sems[4], None)
            forward[1](onward, ins, outs)
            moves.wait_arrivals(first=True)
            onward.start()
            moves.wait_arrivals(first=False)
            onward.wait_arrivals()
            later.append(onward)
        else:
            moves.wait_arrivals()
        if to_sibling:
            passed = _Moves(*sems[-2:], None)
            to_sibling[1](passed, ins, outs)
            passed.start()
            passed.wait_arrivals()
            later.append(passed)
        for mv in later + [moves]:
            mv.wait_sent()

    pl.kernel(launch, mesh=plsc.ScalarSubcoreMesh(axis_name="sequencer", num_cores=1), name=name,
              scratch_types=tuple(exchange.scratch() + relay_scratch),
              compiler_params=pltpu.CompilerParams(collective_id=collective_id))()
    return [o[...] for o in outs[:len(outs) - exchange.n_staging]]


def _row_major_copy(a, name):
    r, c = a.shape
    tr = _row_tile(r)

    def body(a_ref, o_ref):
        o_ref[...] = a_ref[...]

    blk = pl.BlockSpec((tr, c), lambda i: (i, 0))
    return pl.pallas_call(body, name=name, grid=(r // tr,), in_specs=[blk], out_specs=blk,
                          out_shape=jax.ShapeDtypeStruct(a.shape, a.dtype), compiler_params=_params("parallel"))(a)


def _is_chip(x, y, chip):
    return jnp.logical_and(x == chip // 2, y == chip % 2)


def _gather_exchange(from_chip, from_all, split=()):
    n1, n2 = len(from_chip), len(from_all)
    near = CHIP_FLIPS[:2]

    def quarters(t, c, first, count=1):
        n = from_chip[t][1].shape[0] // 4
        return pl.ds((2 * c + first) * n, count * n)

    def build(mv, ins, outs):
        x, y, c = _position()
        me = 2 * x + y
        for t, (chip, _) in enumerate(from_chip):
            if t not in split:
                mv.copy(ins[t], outs[t], cond=_is_chip(x, y, chip))
        for t in range(n2):
            mv.copy(ins[n1 + t], outs[n1 + t].at[me])
        for t in split:
            for first in (True, False):
                for f, (fx, fy) in enumerate(near):
                    px, py = _flip(x, fx), _flip(y, fy)
                    part = quarters(t, c, f if first else 1 - f)
                    mv.send(ins[t].at[part], outs[t].at[part], (px, py, c), landing=outs[t].at[part], first=first,
                            send_if=_is_chip(x, y, from_chip[t][0]), recv_if=_is_chip(px, py, from_chip[t][0]))
        for fx, fy in CHIP_FLIPS:
            px, py = _flip(x, fx), _flip(y, fy)
            peer = (px, py, c)
            for t, (chip, _) in enumerate(from_chip):
                if t not in split:
                    mv.send(ins[t], outs[t], peer, landing=outs[t],
                            send_if=_is_chip(x, y, chip), recv_if=_is_chip(px, py, chip))
            for t in range(n2):
                mv.send(ins[n1 + t], outs[n1 + t].at[me], peer, landing=outs[n1 + t].at[2 * px + py])

    def forward(mv, ins, outs):
        x, y, c = _position()
        for t in split:
            chip = from_chip[t][0]
            for f, (fx, fy) in enumerate(near):
                gx, gy = near[1 - f]
                part = quarters(t, c, f)
                mv.send(outs[t].at[part], outs[t].at[part], (_flip(x, gx), _flip(y, gy), c), landing=outs[t].at[part],
                        send_if=_is_chip(_flip(x, fx), _flip(y, fy), chip), recv_if=_is_chip(1 - x, 1 - y, chip))

    def to_sibling(mv, ins, outs):
        x, y, c = _position()
        for t in split:
            came = jnp.logical_not(_is_chip(x, y, from_chip[t][0]))
            mv.send(outs[t].at[quarters(t, c, 0, 2)], outs[t].at[quarters(t, c, 0, 2)], (x, y, 1 - c),
                    landing=outs[t].at[quarters(t, 1 - c, 0, 2)], send_if=came, recv_if=came)

    arrays = [a for _, a in from_chip] + list(from_all)
    shapes = [jax.ShapeDtypeStruct(a.shape, a.dtype) for _, a in from_chip]
    shapes += [jax.ShapeDtypeStruct((N_CHIPS,) + a.shape, a.dtype) for a in from_all]
    n_remote = len(CHIP_FLIPS) * (n1 - len(split) + n2) + 2 * len(near) * len(split)
    relays = ((len(near) * len(split), forward), (len(split), to_sibling)) if split else None
    return _Exchange(arrays, shapes, n_remote, n1 + n2, build, relays, in_place=split)


def _scatter_exchange(to_chip, to_all, via_neighbours=False):
    n1, n2 = len(to_chip), len(to_all)
    near = CHIP_FLIPS[:2]
    direct = near if via_neighbours else CHIP_FLIPS

    def half(t, g):
        n = to_chip[t][1].shape[0] // 2
        return pl.ds(g * n, n)

    def build(mv, ins, outs):
        x, y, c = _position()
        if via_neighbours:
            for t, (chip, _) in enumerate(to_chip):
                for g, (gx, gy) in enumerate(near):
                    ox, oy = near[1 - g]
                    mv.send(ins[t].at[half(t, g)], outs[n1 + n2 + t], (_flip(x, gx), _flip(y, gy), c),
                            landing=outs[n1 + n2 + t], first=True, send_if=_is_chip(1 - x, 1 - y, chip),
                            recv_if=_is_chip(_flip(x, ox), _flip(y, oy), chip))
        for f, (fx, fy) in enumerate(CHIP_FLIPS):
            px, py = _flip(x, fx), _flip(y, fy)
            peer = (px, py, c)
            if (fx, fy) in direct:
                for t, (chip, _) in enumerate(to_chip):
                    mv.send(ins[t], outs[t].at[f], peer, landing=outs[t].at[f],
                            send_if=_is_chip(px, py, chip), recv_if=_is_chip(x, y, chip))
            for t in range(n2):
                mv.send(ins[n1 + t].at[2 * px + py], outs[n1 + t].at[f], peer, landing=outs[n1 + t].at[f])

    def forward(mv, ins, outs):
        x, y, c = _position()
        for t, (chip, _) in enumerate(to_chip):
            for g in range(len(near)):
                ox, oy = near[1 - g]
                far_slot = outs[t].at[len(near)].at[half(t, g)]
                mv.send(outs[n1 + n2 + t], far_slot, (_flip(x, ox), _flip(y, oy), c), landing=far_slot,
                        send_if=_is_chip(_flip(x, ox), _flip(y, oy), chip), recv_if=_is_chip(x, y, chip))

    arrays = [a for _, a in to_chip] + list(to_all)
    shapes = [jax.ShapeDtypeStruct((len(CHIP_FLIPS),) + a.shape, a.dtype) for _, a in to_chip]
    shapes += [jax.ShapeDtypeStruct((len(CHIP_FLIPS),) + a.shape[1:], a.dtype) for a in to_all]
    if not via_neighbours:
        return _Exchange(arrays, shapes, len(CHIP_FLIPS) * (n1 + n2), 0, build)
    shapes += [jax.ShapeDtypeStruct((a.shape[0] // 2, a.shape[1]), a.dtype) for _, a in to_chip]
    return _Exchange(arrays, shapes, 2 * len(near) * n1 + len(CHIP_FLIPS) * n2, 0, build,
                     relays=((len(near) * n1, forward), None), n_staging=n1)


def _swap_sibling(tensors, name):
    n = len(tensors)

    def body(*refs):
        ins, outs = refs[:n], refs[n:2 * n]
        send_sems, recv_sems = refs[2 * n:]
        x, y, c = _position()
        copies = [pltpu.make_async_remote_copy(
            src_ref=ins[t], dst_ref=outs[t], send_sem=send_sems.at[t], recv_sem=recv_sems.at[t],
            device_id=(x, y, 1 - c), device_id_type=MESH) for t in range(n)]
        for cp in copies:
            cp.start()
        for cp in copies:
            cp.wait_recv()
        for cp in copies:
            cp.wait_send()

    return pl.pallas_call(
        body, name=name, in_specs=[ANY] * n, out_specs=[ANY] * n,
        out_shape=[jax.ShapeDtypeStruct(a.shape, a.dtype) for a in tensors],
        scratch_shapes=[pltpu.SemaphoreType.DMA((n,)), pltpu.SemaphoreType.DMA((n,))],
        compiler_params=pltpu.CompilerParams(has_side_effects=True),
    )(*tensors)


def _pair_halves(g):
    r, cols = g.shape
    half = r // 2

    def body(g_ref, o_ref, mine, theirs, send_sem, recv_sem, local_sem):
        x, y, c = _position()
        away = pltpu.make_async_remote_copy(
            src_ref=g_ref.at[pl.ds((1 - c) * half, half)], dst_ref=theirs, send_sem=send_sem, recv_sem=recv_sem,
            device_id=(x, y, 1 - c), device_id_type=MESH)
        kept = pltpu.make_async_copy(g_ref.at[pl.ds(c * half, half)], mine, local_sem)
        away.start()
        kept.start()
        kept.wait()
        away.wait_recv()
        o_ref[...] = (mine[...] + theirs[...]).astype(BF16)
        away.wait_send()

    return pl.pallas_call(
        body, name="pair_halves", in_specs=[ANY], out_specs=pl.BlockSpec(memory_space=pltpu.VMEM),
        out_shape=jax.ShapeDtypeStruct((half, cols), BF16),
        scratch_shapes=[pltpu.VMEM((half, cols), F32), pltpu.VMEM((half, cols), F32),
                        pltpu.SemaphoreType.DMA(()), pltpu.SemaphoreType.DMA(()), pltpu.SemaphoreType.DMA(())],
        compiler_params=pltpu.CompilerParams(has_side_effects=True, vmem_limit_bytes=VMEM_LIMIT),
    )(g)


def _allreduce_small(slab):
    stages = 3

    def body(x_ref, o_ref, buf, send_sems, recv_sems):
        x, y, c = _position()
        peers = ((1 - x, y, c), (x, 1 - y, c), (x, y, 1 - c))
        o_ref[...] = x_ref[...]
        for k, peer in enumerate(peers):
            cp = pltpu.make_async_remote_copy(src_ref=o_ref, dst_ref=buf.at[k], send_sem=send_sems.at[k],
                                              recv_sem=recv_sems.at[k], device_id=peer, device_id_type=MESH)
            cp.start()
            cp.wait()
            o_ref[...] = o_ref[...] + buf[k]

    return pl.pallas_call(
        body, name="allreduce_small",
        in_specs=[pl.BlockSpec(memory_space=pltpu.VMEM)], out_specs=pl.BlockSpec(memory_space=pltpu.VMEM),
        out_shape=jax.ShapeDtypeStruct(slab.shape, slab.dtype),
        scratch_shapes=[pltpu.VMEM((stages,) + slab.shape, slab.dtype),
                        pltpu.SemaphoreType.DMA((stages,)), pltpu.SemaphoreType.DMA((stages,))],
        compiler_params=pltpu.CompilerParams(has_side_effects=True),
    )(slab)


def _row_tile(r):
    return min(r, 256)


def _sum4(stack, recv, me):
    _, r, c = stack.shape
    tr = _row_tile(r)

    def body(me_ref, own_ref, recv_ref, o_ref):
        o_ref[...] = (((own_ref[...] + recv_ref[0].astype(F32)) + recv_ref[1].astype(F32))
                      + recv_ref[2].astype(F32))

    return pl.pallas_call(
        body, name="sum_partials",
        grid_spec=pltpu.PrefetchScalarGridSpec(
            num_scalar_prefetch=1, grid=(r // tr,),
            in_specs=[pl.BlockSpec((None, tr, c), lambda i, me_ref: (me_ref[0], i, 0)),
                      pl.BlockSpec((len(CHIP_FLIPS), tr, c), lambda i, me_ref: (0, i, 0))],
            out_specs=pl.BlockSpec((tr, c), lambda i, me_ref: (i, 0))),
        out_shape=jax.ShapeDtypeStruct((r, c), F32), compiler_params=_params("parallel"),
    )(me, stack, recv)


def _sum_block(own, recv):
    r, c = own.shape
    tr = _row_tile(r)

    def body(own_ref, recv_ref, o_ref):
        o_ref[...] = (((own_ref[...] + recv_ref[0].astype(F32)) + recv_ref[1].astype(F32))
                      + recv_ref[2].astype(F32))

    return pl.pallas_call(
        body, name="sum_block", grid=(r // tr,),
        in_specs=[pl.BlockSpec((tr, c), lambda i: (i, 0)), pl.BlockSpec((len(CHIP_FLIPS), tr, c), lambda i: (0, i, 0))],
        out_specs=pl.BlockSpec((tr, c), lambda i: (i, 0)),
        out_shape=jax.ShapeDtypeStruct((r, c), F32), compiler_params=_params("parallel"),
    )(own, recv)


def _sum_half(own, recv, core):
    r, c = own.shape
    tr = _row_tile(r // 2)
    per_half = r // 2 // tr

    def body(core_ref, own_ref, recv_ref, o_ref):
        mine = pl.program_id(0) // per_half == core_ref[0]

        @pl.when(mine)
        def _():
            o_ref[...] = (((own_ref[...] + recv_ref[0].astype(F32)) + recv_ref[1].astype(F32))
                          + recv_ref[2].astype(F32))

        @pl.when(jnp.logical_not(mine))
        def _():
            o_ref[...] = own_ref[...]

    return pl.pallas_call(
        body, name="sum_half",
        grid_spec=pltpu.PrefetchScalarGridSpec(
            num_scalar_prefetch=1, grid=(r // tr,),
            in_specs=[pl.BlockSpec((tr, c), lambda i, core: (i, 0)),
                      pl.BlockSpec((len(CHIP_FLIPS), tr, c), lambda i, core: (0, i % per_half, 0))],
            out_specs=pl.BlockSpec((tr, c), lambda i, core: (i, 0))),
        out_shape=jax.ShapeDtypeStruct((r, c), F32), compiler_params=_params("parallel"),
    )(core, own, recv)


def _adamw_math(w, g, m, v):
    m = ADAM_B1 * m + (1.0 - ADAM_B1) * g
    v = ADAM_B2 * v + (1.0 - ADAM_B2) * (g * g)
    m_hat = m / (1.0 - ADAM_B1 ** ADAM_STEP)
    v_hat = v / (1.0 - ADAM_B2 ** ADAM_STEP)
    delta = -ADAM_LR * (m_hat / (jnp.sqrt(v_hat) + ADAM_EPS) + ADAM_WD * w)
    return delta, m, v


def _adamw(w, m, v, g_parts, name):
    r, c = w.shape
    tr = _row_tile(r)
    n = len(g_parts)

    def body(*refs):
        w_ref, m_ref, v_ref = refs[:3]
        g_refs = refs[3:3 + n]
        g_out, d_out, m_out, v_out, zero_out = refs[3 + n:]
        g = g_refs[0][...]
        for ref in g_refs[1:]:
            g = g + ref[...]
        g_out[...] = g
        d_out[...], m_out[...], v_out[...] = _adamw_math(w_ref[...], g, m_ref[...], v_ref[...])
        zero_out[0] = 0

    blk = pl.BlockSpec((tr, c), lambda i: (i, 0))
    return pl.pallas_call(
        body, name=name, grid=(r // tr,), in_specs=[blk] * (3 + n),
        out_specs=[blk] * 4 + [pl.BlockSpec(memory_space=pltpu.SMEM)],
        out_shape=[jax.ShapeDtypeStruct((r, c), F32)] * 4 + [jax.ShapeDtypeStruct((1,), jnp.int32)],
        compiler_params=_params("arbitrary"),
    )(w, m, v, *g_parts)


def _adamw_with_sibling(w, m, v, mine, name):
    r, c = w.shape
    tr = _row_tile(r)
    nb = r // tr

    def body(w_ref, m_ref, v_ref, mine_ref, mine_hbm, g_out, d_out, m_out, v_out, zero_out, theirs, send_sems, recv_sems):
        i = pl.program_id(0)
        x, y, core = _position()

        def tile(k):
            return pltpu.make_async_remote_copy(
                src_ref=mine_hbm.at[pl.ds(pl.multiple_of(k * tr, tr), tr)], dst_ref=theirs.at[k],
                send_sem=send_sems.at[k], recv_sem=recv_sems.at[k], device_id=(x, y, 1 - core), device_id_type=MESH)

        @pl.when(i == 0)
        def _():
            for k in range(nb):
                tile(k).start()

        tile(i).wait_recv()
        g = mine_ref[...] + theirs[i]
        g_out[...] = g
        d_out[...], m_out[...], v_out[...] = _adamw_math(w_ref[...], g, m_ref[...], v_ref[...])
        zero_out[0] = 0

        @pl.when(i == nb - 1)
        def _():
            for k in range(nb):
                tile(k).wait_send()

    blk = pl.BlockSpec((tr, c), lambda i: (i, 0))
    return pl.pallas_call(
        body, name=name, grid=(nb,), in_specs=[blk] * 4 + [ANY],
        out_specs=[blk] * 4 + [pl.BlockSpec(memory_space=pltpu.SMEM)],
        out_shape=[jax.ShapeDtypeStruct((r, c), F32)] * 4 + [jax.ShapeDtypeStruct((1,), jnp.int32)],
        scratch_shapes=[pltpu.VMEM((nb, tr, c), F32), pltpu.SemaphoreType.DMA((nb,)), pltpu.SemaphoreType.DMA((nb,))],
        compiler_params=pltpu.CompilerParams(dimension_semantics=("arbitrary",), has_side_effects=True,
                                             vmem_limit_bytes=VMEM_LIMIT),
    )(w, m, v, mine, mine)


def _adamw_small(total, w, m, v):
    sizes = [w[n].size for n in SMALL]
    flat = lambda d: [d[n].reshape(1, -1) for n in SMALL]
    k = len(SMALL)

    def body(*refs):
        total_ref, w_refs, m_refs, v_refs = refs[0], refs[1:1 + k], refs[1 + k:1 + 2 * k], refs[1 + 2 * k:1 + 3 * k]
        outs = refs[1 + 3 * k:]
        for i, size in enumerate(sizes):
            g = total_ref[i:i + 1, 0:size]
            outs[i][...] = g
            outs[k + i][...], outs[2 * k + i][...], outs[3 * k + i][...] = _adamw_math(
                w_refs[i][...], g, m_refs[i][...], v_refs[i][...])

    res = pl.pallas_call(
        body, name="adamw_small", out_shape=[jax.ShapeDtypeStruct((1, size), F32) for size in sizes] * 4,
        compiler_params=_params(),
    )(total, *flat(w), *flat(m), *flat(v))
    return [{n: res[j * k + i].reshape(w[n].shape) for i, n in enumerate(SMALL)} for j in range(4)]


SHARDED = ("w_in", "w_lora_up", "a_lora_up", "w_out_a", "w_out_b", "w_out")
ROW_SHARDED = ("w_out",)
SMALL = ("norm_g", "shift_mu", "w0", "a0", "k_k", "k_a", "r_k", "lnx_w", "lnx_b", "f_bias", "q_norm_g", "k_norm_g",
         "final_norm_g")
WEIGHTS = ("norm_g", "w_in", "shift_mu", "w_lora_up", "w0", "a_lora_up", "a0", "k_k", "k_a", "r_k", "lnx_w", "lnx_b",
           "f_bias", "q_norm_g", "k_norm_g", "w_out_a", "w_out_b", "w_out", "final_norm_g")
SLAB_ROWS = 16
SLAB_COLS = SEC


def _to_slab(named, extra=None):
    rows = [jnp.pad(named[n].reshape(1, -1), ((0, 0), (0, SLAB_COLS - named[n].size))) for n in SMALL]
    if extra is not None:
        rows.append(jnp.pad(extra.reshape(1, -1), ((0, 0), (0, SLAB_COLS - extra.size))))
    rows.append(jnp.zeros((SLAB_ROWS - len(rows), SLAB_COLS), F32))
    return jnp.concatenate(rows, axis=0)


def _by_chip(g, name):
    if name in ROW_SHARDED:
        return g.reshape(N_CHIPS, g.shape[0] // N_CHIPS, g.shape[1])
    r, c = g.shape
    return g.reshape(r, N_CHIPS, c // N_CHIPS).transpose(1, 0, 2)


def _from_chips(stack, name):
    if name in ROW_SHARDED:
        return stack.reshape(-1, stack.shape[2])
    _, r, c = stack.shape
    return stack.transpose(1, 0, 2).reshape(r, N_CHIPS * c)


def kernel(x, norm_g, w_in, shift_mu, w_lora_up, w0, a_lora_up, a0, k_k, k_a, r_k, lnx_w, lnx_b, f_bias, q_norm_g, k_norm_g, w_out_a, w_out_b, w_out, final_norm_g, loss_target, m_norm_g, m_w_in, m_shift_mu, m_w_lora_up, m_w0, m_a_lora_up, m_a0, m_k_k, m_k_a, m_r_k, m_lnx_w, m_lnx_b, m_f_bias, m_q_norm_g, m_k_norm_g, m_w_out_a, m_w_out_b, m_w_out, m_final_norm_g, v_norm_g, v_w_in, v_shift_mu, v_w_lora_up, v_w0, v_a_lora_up, v_a0, v_k_k, v_k_a, v_r_k, v_lnx_w, v_lnx_b, v_f_bias, v_q_norm_g, v_k_norm_g, v_w_out_a, v_w_out_b, v_w_out, v_final_norm_g):
    w = dict(norm_g=norm_g, w_in=w_in, shift_mu=shift_mu, w_lora_up=w_lora_up, w0=w0, a_lora_up=a_lora_up, a0=a0,
             k_k=k_k, k_a=k_a, r_k=r_k, lnx_w=lnx_w, lnx_b=lnx_b, f_bias=f_bias, q_norm_g=q_norm_g,
             k_norm_g=k_norm_g, w_out_a=w_out_a, w_out_b=w_out_b, w_out=w_out, final_norm_g=final_norm_g)
    m = dict(norm_g=m_norm_g, w_in=m_w_in, shift_mu=m_shift_mu, w_lora_up=m_w_lora_up, w0=m_w0,
             a_lora_up=m_a_lora_up, a0=m_a0, k_k=m_k_k, k_a=m_k_a, r_k=m_r_k, lnx_w=m_lnx_w, lnx_b=m_lnx_b,
             f_bias=m_f_bias, q_norm_g=m_q_norm_g, k_norm_g=m_k_norm_g, w_out_a=m_w_out_a, w_out_b=m_w_out_b,
             w_out=m_w_out, final_norm_g=m_final_norm_g)
    v = dict(norm_g=v_norm_g, w_in=v_w_in, shift_mu=v_shift_mu, w_lora_up=v_w_lora_up, w0=v_w0,
             a_lora_up=v_a_lora_up, a0=v_a0, k_k=v_k_k, k_a=v_k_a, r_k=v_r_k, lnx_w=v_lnx_w, lnx_b=v_lnx_b,
             f_bias=v_f_bias, q_norm_g=v_q_norm_g, k_norm_g=v_k_norm_g, w_out_a=v_w_out_a, w_out_b=v_w_out_b,
             w_out=v_w_out, final_norm_g=v_final_norm_g)
    shapes = {n: w[n].shape for n in WEIGHTS}

    shard = {n: w[n][0].astype(BF16) for n in SHARDED}
    late = ("w_out_a", "w_out_b", "w_out")
    loras = ("w_lora_up", "a_lora_up")
    w_in_head, w_in_tail = shard["w_in"][:, :A_TAIL], shard["w_in"][:, A_TAIL:]
    shard0, shard1_head, up_stack, aup_stack = _run_on_sequencer(_gather_exchange(
        [(0, shard["w_in"]), (1, w_in_head)], [shard[n] for n in loras], split=(0, 1)), "gather_early", 2)
    moments = (_row_major_copy(m["w_in"][0], "m_w_in_rows"), _row_major_copy(v["w_in"][0], "v_w_in_rows"))
    shard0, moments = lax.optimization_barrier((shard0, moments))
    w_a = jnp.concatenate([shard0, shard1_head], axis=1)

    def late_weights(arrived):
        shard1_tail, shard2, shard3 = arrived[:3]
        w_b = jnp.concatenate([shard1_tail, shard2[:, :B_TAIL], jnp.zeros((D_MODEL, SEC - FOX_REAL), BF16)], axis=1)
        w_g = jnp.concatenate([shard2[:, B_TAIL:], shard3], axis=1)
        return (w_b, w_g, *[_from_chips(s, n) for n, s in zip(late, arrived[3:])])

    own = {}
    cut = {"block0": lambda: own["dw_a"][:, :SHARD_COLS], "head1": lambda: own["dw_a"][:, SHARD_COLS:],
           "tail1": lambda: own["dw_b"][:, :B_HEAD],
           "block2": lambda: jnp.concatenate([own["dw_b"][:, B_HEAD:FOX_REAL], own["dw_g"][:, :G_HEAD]], axis=1),
           "block3": lambda: own["dw_g"][:, G_HEAD:]}

    def bwd_exchange(dw_b, dw_g, dwa, dwb, dwo):
        own.update(dw_b=dw_b, dw_g=dw_g)
        own.update({n: _by_chip(g, n) for n, g in zip(late, (dwa, dwb, dwo))})
        return _scatter_exchange([(1, cut["tail1"]().astype(BF16)), (2, cut["block2"]().astype(BF16)),
                                  (3, cut["block3"]().astype(BF16))], [own[n].astype(BF16) for n in late])

    def tail_exchange(dw_a, dw_up, da_up):
        own.update(dw_a=dw_a)
        own.update({n: _by_chip(g, n) for n, g in zip(loras, (dw_up, da_up))})
        pair = _pair_halves(dw_a)
        return _scatter_exchange([(0, pair[:, :SHARD_COLS]), (1, pair[:, SHARD_COLS:])],
                                 [own[n].astype(BF16) for n in loras], via_neighbours=True)

    small = {n: w[n] for n in SMALL}
    loss_vec, grad_x, grads, sent, sent_last = _device_grads(
        x[0], loss_target[0], small, w_a, _from_chips(up_stack, "w_lora_up"), _from_chips(aup_stack, "a_lora_up"),
        late_weights, _gather_exchange([(1, w_in_tail), (2, shard["w_in"]), (3, shard["w_in"])], [shard[n] for n in late]),
        bwd_exchange, tail_exchange)

    total = _allreduce_small(_to_slab(grads, extra=loss_vec))
    loss = (0.5 / D_MODEL) * jnp.sum(total[len(SMALL)])
    out_g, out_d, out_m, out_v = _adamw_small(total, w, m, v)

    xpos, ypos, cpos = _position()
    me = (2 * xpos + ypos).astype(jnp.int32).reshape(1)
    core = cpos.astype(jnp.int32).reshape(1)
    core_sum, theirs = {}, {}

    def update(n):
        if n == "w_in":
            g, d, m2, v2, zero = _adamw_with_sibling(w[n][0], *moments, core_sum[n], "adamw_" + n)
        else:
            g, d, m2, v2, zero = _adamw(w[n][0], m[n][0], v[n][0], [core_sum[n], theirs[n]], "adamw_" + n)
        out_g[n], out_d[n], out_m[n], out_v[n] = (a.reshape(shapes[n]) for a in (g, d, m2, v2))
        return zero

    sent_last, out_d["norm_g"] = lax.optimization_barrier((sent_last, out_d["norm_g"]))
    core_sum["w_in"] = lax.switch(me[0], [
        lambda: _sum_half(cut["block0"](), sent_last[0], core),
        lambda: jnp.concatenate([_sum_half(cut["head1"](), sent_last[1], core), _sum_block(cut["tail1"](), sent[0])],
                                axis=1),
        lambda: _sum_block(cut["block2"](), sent[1]),
        lambda: _sum_block(cut["block3"](), sent[2])])
    after_w_in = me + update("w_in")
    others = loras + late
    core_sum.update({n: _sum4(own[n], r, after_w_in) for n, r in zip(others, sent_last[2:] + sent[3:])})
    theirs.update(zip(others, _swap_sibling([core_sum[n] for n in others], "swap_sibling_late")))
    for n in others:
        update(n)

    return (loss, grad_x.reshape(x.shape), *[out_g[n] for n in WEIGHTS], *[out_d[n] for n in WEIGHTS],
            *[out_m[n] for n in WEIGHTS], *[out_v[n] for n in WEIGHTS])
```

```python
import functools
import math

import jax
import jax.numpy as jnp
from jax import lax
from jax.experimental import pallas as pl
from jax.experimental.pallas import tpu as pltpu
from jax.experimental.pallas import tpu_sc as plsc

F32 = jnp.float32
BF16 = jnp.bfloat16

D_MODEL = 1024
D_HALF = 512
HEAD = 64
N_HEADS = 8
LORA = 64
RWKV_COLS = 2176
FOX_REAL = 2056
SEC = 2176
GATE_COLS = 2048
IN_COLS = 6280
N_CHIPS = 4
SHARD_COLS = IN_COLS // N_CHIPS
A_TAIL = RWKV_COLS - SHARD_COLS
B_HEAD = SHARD_COLS - A_TAIL
B_TAIL = FOX_REAL - B_HEAD
G_HEAD = SHARD_COLS - B_TAIL
RMS_EPS = 1e-6
LNX_EPS = 64e-5
ATT_SCALE = HEAD ** -0.5
NEG = -1e30

ADAM_LR = 0.001
ADAM_B1 = 0.9
ADAM_B2 = 0.999
ADAM_EPS = 1e-08
ADAM_WD = 0.01
ADAM_STEP = 10

LANES = 128
SUBLANES = 8
VMEM_LIMIT = 56 * 1024 * 1024
MESH = pl.DeviceIdType.MESH


def _params(*sem):
    return pltpu.CompilerParams(dimension_semantics=sem if sem else None, vmem_limit_bytes=VMEM_LIMIT)


def _sigmoid(x):
    return 1.0 / (1.0 + jnp.exp(-x))


def _log_sigmoid(x):
    return jnp.minimum(x, 0.0) - jnp.log(1.0 + jnp.exp(-jnp.abs(x)))


def _head_ones():
    r = lax.broadcasted_iota(jnp.int32, (LANES, LANES), 0) >> 6
    c = lax.broadcasted_iota(jnp.int32, (LANES, LANES), 1) >> 6
    return (r == c).astype(BF16)


def _split3(x):
    hi = x.astype(BF16)
    r1 = x - hi.astype(F32)
    mid = r1.astype(BF16)
    lo = (r1 - mid.astype(F32)).astype(BF16)
    return hi, mid, lo


def _exact_dot(x, ones_bf16, ones_first=False):
    out = None
    for piece in _split3(x):
        if ones_first:
            t = jnp.dot(ones_bf16, piece, preferred_element_type=F32)
        else:
            t = jnp.dot(piece, ones_bf16, preferred_element_type=F32)
        out = t if out is None else out + t
    return out


def _head_sum(x, bd):
    n = x.shape[1] // LANES
    parts = [_exact_dot(x[:, i * LANES:(i + 1) * LANES], bd) for i in range(n)]
    return parts[0] if n == 1 else jnp.concatenate(parts, axis=1)


def _dot_nt(a, b):
    return lax.dot_general(a, b, (((1,), (1,)), ((), ())), preferred_element_type=F32)


def _dot_tn(a, b):
    return lax.dot_general(a, b, (((0,), (0,)), ((), ())), preferred_element_type=F32)


def _colsum(x):
    return jnp.sum(x, axis=0, keepdims=True)


def _matmul_tn_acc(at, b, name, tk=512):
    m, k = at.shape
    n = b.shape[1]

    def body(a_ref, b_ref, o_ref):
        j = pl.program_id(0)

        @pl.when(j == 0)
        def _():
            o_ref[...] = jnp.zeros_like(o_ref)

        o_ref[...] += jnp.dot(a_ref[...], b_ref[...].astype(BF16), preferred_element_type=F32)

    return pl.pallas_call(
        body, name=name, grid=(k // tk,),
        in_specs=[pl.BlockSpec((m, tk), lambda j: (0, j)), pl.BlockSpec((tk, n), lambda j: (j, 0))],
        out_specs=pl.BlockSpec((m, n), lambda j: (0, 0)),
        out_shape=jax.ShapeDtypeStruct((m, n), F32), compiler_params=_params("arbitrary"),
    )(at, b)


def _inproj_bwd(du_a, du_b, du_g, w_a, w_b, w_g, x, dx2, g, exchange=None, tm=256):
    s, d = x.shape
    nb = s // tm

    def body(*refs):
        ((da_ref, db_ref, dg_ref, wa_ref, wb_ref, wg_ref, x_ref, dx2_ref, g_ref), (gx_ref, gg_ref), _,
         moves) = _split_refs(refs, 9, 2, exchange)
        i = pl.program_id(0)
        if moves:
            moves.start(also=(i == 0))

        @pl.when(i == 0)
        def _():
            gg_ref[...] = jnp.zeros_like(gg_ref)

        dh = _dot_nt(da_ref[...].astype(BF16), wa_ref[...])
        dh += _dot_nt(db_ref[...].astype(BF16), wb_ref[...])
        dh += _dot_nt(dg_ref[...].astype(BF16), wg_ref[...])
        xv = x_ref[...]
        r = lax.rsqrt(jnp.mean(xv * xv, axis=-1, keepdims=True) + RMS_EPS)
        xh = xv * r
        gg_ref[...] += _colsum(dh * xh)
        dxh = dh * g_ref[...]
        gx_ref[...] = dx2_ref[...] + r * (dxh - xh * jnp.mean(dxh * xh, axis=-1, keepdims=True))
        if moves:
            moves.wait(also=(i == nb - 1))

    row = lambda w: pl.BlockSpec((tm, w), lambda i: (i, 0))
    full = lambda a: pl.BlockSpec(a.shape, lambda i: (0, 0))
    ex_in = exchange.operands if exchange else []
    ex_out = exchange.out_shapes if exchange else []
    res = pl.pallas_call(
        body, name="inproj_bwd", grid=(nb,),
        in_specs=[row(SEC), row(SEC), row(GATE_COLS), full(w_a), full(w_b), full(w_g), row(d), row(d), full(g)]
                 + [ANY] * len(ex_in),
        out_specs=[row(d), pl.BlockSpec((1, d), lambda i: (0, 0))] + [ANY] * len(ex_out),
        out_shape=[jax.ShapeDtypeStruct((s, d), F32), jax.ShapeDtypeStruct((1, d), F32)] + ex_out,
        scratch_shapes=exchange.scratch() if exchange else [],
        compiler_params=_params("arbitrary"),
    )(du_a, du_b, du_g, w_a, w_b, w_g, x, dx2, g, *ex_in)
    return res[0], res[1], list(res[2:])


def _rwkv_elementwise(ua, prev_row, first, mu, wl, w0, a0, kkw, kaw, bd):
    tm = ua.shape[0]
    rows = lax.broadcasted_iota(jnp.int32, (tm, 1), 0)
    prev = jnp.where(first, jnp.zeros_like(prev_row), prev_row)
    shifted = jnp.where(rows == 0, prev, pltpu.roll(ua, 1, 0))
    delta = shifted - ua
    us = ua + delta * mu
    r = us[:, 0:512]
    k0 = us[:, 512:1024]
    v = us[:, 1024:1536]
    lo = us[:, 1536:1664]
    gate = us[:, 1664:2176]
    lane = lax.broadcasted_iota(jnp.int32, (1, LANES), 1)
    th = jnp.tanh(lo)
    lin = jnp.where(lane < LORA, th, lo)
    ll = jnp.dot(lin.astype(BF16), wl, preferred_element_type=F32)
    sz = _sigmoid(w0 + ll[:, :512])
    e = sz * math.exp(-0.5)
    dec = jnp.exp(-e)
    a = _sigmoid(a0 + ll[:, 512:])
    kk0 = k0 * kkw
    ss = _head_sum(kk0 * kk0, bd)
    nrm = jnp.maximum(jnp.sqrt(ss), 1e-12)
    kk = kk0 / nrm
    k = k0 * (1.0 + (a - 1.0) * kaw)
    return dict(delta=delta, us=us, r=r, k0=k0, v=v, lo=lo, gate=gate, th=th, lin=lin, sz=sz, e=e, dec=dec,
                a=a, kk0=kk0, ss=ss, nrm=nrm, kk=kk, k=k)


def _rwkv_front(x, g, w_a, mu, wl, w0, a0, kkw, kaw, tm=256):
    s, d = x.shape

    def body(x_ref, g_ref, wa_ref, mu_ref, wl_ref, w0_ref, a0_ref, kkw_ref, kaw_ref,
             h_ref, ua_ref, r_ref, w_ref, k_ref, v_ref, a_ref, b_ref, gate_ref, last_row):
        i = pl.program_id(0)
        xv = x_ref[...]
        h = (xv * lax.rsqrt(jnp.mean(xv * xv, axis=-1, keepdims=True) + RMS_EPS) * g_ref[...]).astype(BF16)
        h_ref[...] = h
        ua = jnp.dot(h, wa_ref[...], preferred_element_type=F32)
        ua_ref[...] = ua

        @pl.when(i == 0)
        def _():
            last_row[...] = jnp.zeros_like(last_row)

        f = _rwkv_elementwise(ua, last_row[...], i == 0, mu_ref[...], wl_ref[...], w0_ref[...],
                              a0_ref[...], kkw_ref[...], kaw_ref[...], _head_ones())
        last_row[...] = ua[tm - 1:tm, :]
        r_ref[...] = f["r"]
        w_ref[...] = f["dec"]
        k_ref[...] = f["k"]
        v_ref[...] = f["v"]
        a_ref[...] = -f["kk"]
        b_ref[...] = f["kk"] * f["a"]
        gate_ref[...] = f["gate"]

    vec = lambda w: pl.BlockSpec((1, w), lambda i: (0, 0))
    row = lambda w: pl.BlockSpec((tm, w), lambda i: (i, 0))
    return pl.pallas_call(
        body, name="rwkv_front", grid=(s // tm,),
        in_specs=[row(d), vec(d), pl.BlockSpec(w_a.shape, lambda i: (0, 0), pipeline_mode=pl.Buffered(1)),
                  vec(SEC), pl.BlockSpec((LANES, 2 * D_HALF), lambda i: (0, 0)),
                  vec(D_HALF), vec(D_HALF), vec(D_HALF), vec(D_HALF)],
        out_specs=[row(d), row(SEC)] + [row(D_HALF)] * 7,
        out_shape=[jax.ShapeDtypeStruct((s, d), BF16), jax.ShapeDtypeStruct((s, SEC), F32)]
                  + [jax.ShapeDtypeStruct((s, D_HALF), F32)] * 7,
        scratch_shapes=[pltpu.VMEM((1, SEC), F32)],
        compiler_params=_params("arbitrary"),
    )(x, g, w_a, mu, wl, w0, a0, kkw, kaw)


SCAN_TB = 128
N_PAIRS = 4


def _pair_sum(x, left):
    s_l = jnp.sum(jnp.where(left, x, 0.0), axis=1, keepdims=True)
    s_r = jnp.sum(jnp.where(left, 0.0, x), axis=1, keepdims=True)
    return jnp.where(left, s_l, s_r)


def _pair_dot(x, row_l, row_r, left):
    s_l = jnp.sum(x * row_l, axis=1, keepdims=True)
    s_r = jnp.sum(x * row_r, axis=1, keepdims=True)
    return jnp.where(left, s_l, s_r)


def _halves(rows8):
    lane = lax.broadcasted_iota(jnp.int32, rows8.shape, 1)
    keep_left = (lane & (LANES - 1)) < HEAD
    return jnp.where(keep_left, rows8, 0.0), jnp.where(keep_left, 0.0, rows8)


def _quad_consts():
    lane = lax.broadcasted_iota(jnp.int32, (HEAD, 2 * LANES), 1)
    rowi = lax.broadcasted_iota(jnp.int32, (HEAD, 2 * LANES), 0)
    diag2 = rowi == (lane & (HEAD - 1))
    r = lax.broadcasted_iota(jnp.int32, (2 * LANES, 2 * LANES), 0) >> 6
    c = lax.broadcasted_iota(jnp.int32, (2 * LANES, 2 * LANES), 1) >> 6
    return diag2, (r == c).astype(BF16)


def _rows_to_columns(x8, diag2, bd2):
    lhs = jnp.concatenate([jnp.where(diag2, x8[i:i + 1], 0.0).astype(BF16) for i in range(SUBLANES)], axis=0)
    return jnp.dot(lhs, bd2, preferred_element_type=F32)


def _diag_rows(qtile, diag2, bd2, sub_row2):
    res = jnp.dot(qtile, bd2, preferred_element_type=F32)
    out = jnp.zeros((SUBLANES, 2 * LANES), F32)
    for i in range(SUBLANES):
        out = jnp.where(sub_row2 == i, _colsum(jnp.where(diag2, res[i * HEAD:(i + 1) * HEAD], 0.0)), out)
    return out


def _store_tile(qbuf, slot, p, i, x):
    qbuf[slot, p // 2, i * HEAD:(i + 1) * HEAD, (p % 2) * LANES:(p % 2 + 1) * LANES] = x.astype(BF16)


def _left_half():
    return lax.broadcasted_iota(jnp.int32, (HEAD, LANES), 1) < HEAD


def _split_refs(refs, n_rows, n_out, exchange):
    n_in = len(exchange.operands) if exchange else 0
    n_ex_out = len(exchange.out_shapes) if exchange else 0
    refs = list(refs)
    rows, refs = refs[:n_rows], refs[n_rows:]
    ex_in, refs = refs[:n_in], refs[n_in:]
    outs, refs = refs[:n_out], refs[n_out:]
    ex_out, refs = refs[:n_ex_out], refs[n_ex_out:]
    scratch, sems = (refs[:-3], refs[-3:]) if exchange else (refs, None)
    moves = exchange.moves(ex_in, ex_out, sems) if exchange else None
    return rows, outs, scratch, moves


def _wkv_fwd(r, w, k, a, b, v, exchange=None):
    s = r.shape[0]
    tb = SCAN_TB
    nb = s // tb

    def body(*refs):
        (r_ref, w_ref, k_ref, a_ref, b_ref, v_ref), (y_ref, st_ref), (state, vbuf, qbuf), moves = _split_refs(
            refs, 6, 2, exchange)
        g = pl.program_id(0)
        if moves:
            moves.start(also=(g == 0))

        @pl.when(g == 0)
        def _():
            state[...] = jnp.zeros_like(state)
            qbuf[...] = jnp.zeros_like(qbuf)

        left = _left_half()
        diag2, bd2 = _quad_consts()
        sub_row2 = lax.broadcasted_iota(jnp.int32, (SUBLANES, 2 * LANES), 0)
        groups = tb // SUBLANES
        quads = [slice(g2 * 2 * LANES, (g2 + 1) * 2 * LANES) for g2 in range(2)]

        def rows_of(q):
            return pl.ds(pl.multiple_of(q * SUBLANES, SUBLANES), SUBLANES)

        def v_tiles(q, slot):
            v8 = v_ref[rows_of(q), :]
            for g2 in range(2):
                vbuf[slot, g2] = _rows_to_columns(v8[:, quads[g2]], diag2, bd2)

        def chain(q, slot):
            rows8 = rows_of(q)
            a8, w8, b8, k8, r8 = (x[rows8, :] for x in (a_ref, w_ref, b_ref, k_ref, r_ref))
            pairs = [slice(p * LANES, (p + 1) * LANES) for p in range(N_PAIRS)]
            a_next = pltpu.roll(a8, SUBLANES - 1, 0)
            (a8_l, a8_r), (wa8_l, wa8_r) = _halves(a8), _halves(w8 * a_next)
            ba8 =jnp.concatenate([_pair_sum(b8[:, pr] * a_next[:, pr], left[0:SUBLANES]) for pr in pairs], axis=1)
            ka8 = jnp.concatenate([_pair_sum(k8[:, pr] * a_next[:, pr], left[0:SUBLANES]) for pr in pairs], axis=1)
            sp = [state[p] for p in range(N_PAIRS)]
            for i in range(0, SUBLANES, 2):
                r0, r1 = slice(i, i + 1), slice(i + 1, i + 2)
                sums = [(_pair_dot(sp[p], a8_l[r0, pairs[p]], a8_r[r0, pairs[p]], left),
                         _pair_dot(sp[p], wa8_l[r0, pairs[p]], wa8_r[r0, pairs[p]], left)) for p in range(N_PAIRS)]
                sa0, sa1 = [s[0] for s in sums], [s[1] for s in sums]
                for p in range(N_PAIRS):
                    pr = pairs[p]
                    inner = slice((p % 2) * LANES, (p % 2 + 1) * LANES)
                    vt0 = vbuf[slot, p // 2, i * HEAD:(i + 1) * HEAD, inner]
                    vt1 = vbuf[slot, p // 2, (i + 1) * HEAD:(i + 2) * HEAD, inner]
                    sa_next = sa1[p] + sa0[p] * ba8[r0, pr] + vt0 * ka8[r0, pr]
                    s1 = sp[p] * w8[r0, pr] + sa0[p] * b8[r0, pr] + vt0 * k8[r0, pr]
                    st_ref[q * SUBLANES + i, p] = s1
                    _store_tile(qbuf, slot, p, i, s1 * r8[r0, pr])
                    s2 = s1 * w8[r1, pr] + sa_next * b8[r1, pr] + vt1 * k8[r1, pr]
                    st_ref[q * SUBLANES + i + 1, p] = s2
                    _store_tile(qbuf, slot, p, i + 1, s2 * r8[r1, pr])
                    sp[p] = s2
            for p in range(N_PAIRS):
                state[p] = sp[p]

        def y_rows(q, slot):
            for g2 in range(2):
                y_ref[rows_of(q), quads[g2]] = _diag_rows(qbuf[slot, g2], diag2, bd2, sub_row2)

        v_tiles(0, 0)

        def two_groups(j, carry):
            q0 = 2 * j
            v_tiles(q0 + 1, 1)
            chain(q0, 0)
            y_rows(jnp.maximum(q0 - 1, 0), 1)
            v_tiles(jnp.minimum(q0 + 2, groups - 1), 0)
            chain(q0 + 1, 1)
            y_rows(q0, 0)
            return carry

        lax.fori_loop(0, groups // 2, two_groups, 0)
        y_rows(groups - 1, 1)
        if moves:
            moves.wait(also=(g == nb - 1))

    rows = pl.BlockSpec((tb, D_HALF), lambda g: (g, 0))
    ex_in = exchange.operands if exchange else []
    ex_out = exchange.out_shapes if exchange else []
    res = pl.pallas_call(
        body, name="wkv_fwd", grid=(nb,),
        in_specs=[rows] * 6 + [ANY] * len(ex_in),
        out_specs=[rows, pl.BlockSpec((tb, N_PAIRS, HEAD, LANES), lambda g: (g, 0, 0, 0))] + [ANY] * len(ex_out),
        out_shape=[jax.ShapeDtypeStruct((s, D_HALF), F32),
                   jax.ShapeDtypeStruct((s, N_PAIRS, HEAD, LANES), F32)] + ex_out,
        scratch_shapes=[pltpu.VMEM((N_PAIRS, HEAD, LANES), F32),
                        pltpu.VMEM((2, 2, SUBLANES * HEAD, 2 * LANES), F32),
                        pltpu.VMEM((2, 2, SUBLANES * HEAD, 2 * LANES), BF16)]
                       + (exchange.scratch() if exchange else []),
        compiler_params=_params("arbitrary"),
    )(r, w, k, a, b, v, *ex_in)
    return res[0], res[1], list(res[2:])


def _wkv_bwd(r, w, k, a, b, v, dy, st, exchange=None):
    s = r.shape[0]
    tb = SCAN_TB
    nb = s // tb

    def body(*refs):
        ((r_ref, w_ref, k_ref, a_ref, b_ref, v_ref, dy_ref, st_ref, before_ref),
         (dr_ref, dw_ref, dk_ref, dv_ref, da_ref, db_ref), (dstate, vbuf, qbuf, sbuf),
         moves) = _split_refs(refs, 9, 6, exchange)
        g = pl.program_id(0)
        first_block = g == nb - 1
        if moves:
            moves.start(also=(g == 0))

        @pl.when(g == 0)
        def _():
            dstate[...] = jnp.zeros_like(dstate)
            qbuf[...] = jnp.zeros_like(qbuf)

        left = _left_half()
        diag2, bd2 = _quad_consts()
        sub_row = lax.broadcasted_iota(jnp.int32, (SUBLANES, LANES), 0)
        sub_row2 = lax.broadcasted_iota(jnp.int32, (SUBLANES, 2 * LANES), 0)
        groups = tb // SUBLANES
        quads = [slice(g2 * 2 * LANES, (g2 + 1) * 2 * LANES) for g2 in range(2)]
        row_refs = (dr_ref, dw_ref, dk_ref, da_ref, db_ref)

        def rows_of(q):
            return pl.ds(pl.multiple_of(q * SUBLANES, SUBLANES), SUBLANES)

        def state_before(q, i, p):
            if i > 0:
                return st_ref[q * SUBLANES + i - 1, p]
            return jnp.where(q == 0, jnp.where(first_block, 0.0, before_ref[0, p]),
                             st_ref[jnp.maximum(q * SUBLANES - 1, 0), p])

        def column_tiles(q, slot):
            rows8 = rows_of(q)
            for kind, ref in enumerate((v_ref, dy_ref)):
                x8 = ref[rows8, :]
                for g2 in range(2):
                    vbuf[slot, kind, g2] = _rows_to_columns(x8[:, quads[g2]], diag2, bd2)
            a8 = a_ref[rows8, :]
            for i in range(SUBLANES):
                for p in range(N_PAIRS):
                    _store_tile(sbuf, 0, p, i, state_before(q, i, p) * a8[i:i + 1, p * LANES:(p + 1) * LANES])
            for g2 in range(2):
                vbuf[slot, 2, g2] = jnp.dot(sbuf[0, g2], bd2, preferred_element_type=F32)

        def chain(q, slot):
            rows8 = rows_of(q)
            a8, w8, b8, k8, r8 = (x[rows8, :] for x in (a_ref, w_ref, b_ref, k_ref, r_ref))
            b8_l, b8_r = _halves(b8)
            dsp = [dstate[p] for p in range(N_PAIRS)]
            outs = [[jnp.zeros((SUBLANES, LANES), F32) for _ in row_refs] for _ in range(N_PAIRS)]
            after = [st_ref[q * SUBLANES + SUBLANES - 1, p] for p in range(N_PAIRS)]
            for i in reversed(range(SUBLANES)):
                row = slice(i, i + 1)
                pl_ = [slice(p * LANES, (p + 1) * LANES) for p in range(N_PAIRS)]
                tile = [(p // 2, slice(i * HEAD, (i + 1) * HEAD), slice((p % 2) * LANES, (p % 2 + 1) * LANES))
                        for p in range(N_PAIRS)]
                sp = [state_before(q, i, p) for p in range(N_PAIRS)]
                dyt = [vbuf[(slot, 1) + tile[p]] for p in range(N_PAIRS)]
                ds = [dsp[p] + dyt[p] * r8[row, pl_[p]] for p in range(N_PAIRS)]
                dsa = [_pair_dot(ds[p], b8_l[row, pl_[p]], b8_r[row, pl_[p]], left) for p in range(N_PAIRS)]
                sa = [vbuf[(slot, 2) + tile[p]] for p in range(N_PAIRS)]
                for p in range(N_PAIRS):
                    ar, wr, br, kr = (x[row, pl_[p]] for x in (a8, w8, b8, k8))
                    vt = vbuf[(slot, 0) + tile[p]]
                    dsp[p] = ds[p] * wr + dsa[p] * ar
                    new = (_colsum(after[p] * dyt[p]), _colsum(ds[p] * sp[p]), _colsum(ds[p] * vt),
                           _colsum(sp[p] * dsa[p]), _colsum(ds[p] * sa[p]))
                    outs[p] = [jnp.where(sub_row == i, n, o) for n, o in zip(new, outs[p])]
                    _store_tile(qbuf, slot, p, i, ds[p] * kr)
                after = sp
            for p in range(N_PAIRS):
                dstate[p] = dsp[p]
                for ref, o in zip(row_refs, outs[p]):
                    ref[rows8, p * LANES:(p + 1) * LANES] = o

        def dv_rows(q, slot):
            for g2 in range(2):
                dv_ref[rows_of(q), quads[g2]] = _diag_rows(qbuf[slot, g2], diag2, bd2, sub_row2)

        column_tiles(groups - 1, 0)

        def two_groups(j, carry):
            q0 = groups - 1 - 2 * j
            column_tiles(q0 - 1, 1)
            chain(q0, 0)
            dv_rows(jnp.minimum(q0 + 1, groups - 1), 1)
            column_tiles(jnp.maximum(q0 - 2, 0), 0)
            chain(q0 - 1, 1)
            dv_rows(q0, 0)
            return carry

        lax.fori_loop(0, groups // 2, two_groups, 0)
        dv_rows(0, 1)
        if moves:
            moves.wait(also=(g == nb - 1))

    rows = pl.BlockSpec((tb, D_HALF), lambda g: (nb - 1 - g, 0))
    ex_in = exchange.operands if exchange else []
    ex_out = exchange.out_shapes if exchange else []
    res = pl.pallas_call(
        body, name="wkv_bwd", grid=(nb,),
        in_specs=[rows] * 7 + [pl.BlockSpec((tb, N_PAIRS, HEAD, LANES), lambda g: (nb - 1 - g, 0, 0, 0)),
                               pl.BlockSpec((1, N_PAIRS, HEAD, LANES),
                                            lambda g: (jnp.maximum((nb - 1 - g) * tb - 1, 0), 0, 0, 0))]
                 + [ANY] * len(ex_in),
        out_specs=[rows] * 6 + [ANY] * len(ex_out),
        out_shape=[jax.ShapeDtypeStruct((s, D_HALF), F32)] * 6 + ex_out,
        scratch_shapes=[pltpu.VMEM((N_PAIRS, HEAD, LANES), F32),
                        pltpu.VMEM((2, 3, 2, SUBLANES * HEAD, 2 * LANES), F32),
                        pltpu.VMEM((2, 2, SUBLANES * HEAD, 2 * LANES), BF16),
                        pltpu.VMEM((1, 2, SUBLANES * HEAD, 2 * LANES), BF16)]
                       + (exchange.scratch() if exchange else []),
        compiler_params=_params("arbitrary"),
    )(r, w, k, a, b, v, dy, st, st, *ex_in)
    return list(res[:6]), list(res[6:])


def _rwkv_post_math(y, r, k, v, gate, lw, lb, rk, bd):
    mean = _head_sum(y, bd) * (1.0 / HEAD)
    yc = y - mean
    var = _head_sum(yc * yc, bd) * (1.0 / HEAD)
    rstd = lax.rsqrt(var + LNX_EPS)
    yn = yc * rstd
    rkk = _head_sum(r * k * rk, bd)
    sg = _sigmoid(gate)
    pre = yn * lw + lb + rkk * v
    return yn, rstd, rkk, sg, pre


def _rwkv_prep_bwd(u_a, h_t, grads, mu, wl, w0, a0, kkw, kaw, tm=256):
    s = u_a.shape[0]
    nb = s // tm
    d = h_t.shape[0]

    def body(ua_ref, prev_ref, ht_ref, drs_ref, dws_ref, dks_ref, dvs_ref, das_ref, dbs_ref, drb_ref, dkb_ref, dvb_ref,
             dgt_ref, mu_ref, wl_ref, w0_ref, a0_ref, kkw_ref, kaw_ref,
             du_ref, dwa_ref, dmu_ref, dwl_ref, dw0_ref, da0_ref, dkkw_ref, dkaw_ref, carry):
        i = pl.program_id(0)

        @pl.when(i == 0)
        def _():
            carry[...] = jnp.zeros_like(carry)
            for ref in (dwa_ref, dmu_ref, dwl_ref, dw0_ref, da0_ref, dkkw_ref, dkaw_ref):
                ref[...] = jnp.zeros_like(ref)

        bd = _head_ones()
        mu_v, wl_v, kkw_v, kaw_v = mu_ref[...], wl_ref[...], kkw_ref[...], kaw_ref[...]
        f = _rwkv_elementwise(ua_ref[...], prev_ref[7:8, :], i == nb - 1, mu_v, wl_v, w0_ref[...],
                              a0_ref[...], kkw_v, kaw_v, bd)
        a, kk, k0 = f["a"], f["kk"], f["k0"]
        dk = dks_ref[...] + dkb_ref[...]
        dbs = dbs_ref[...]
        dkk = dbs * a - das_ref[...]
        da = dbs * kk + dk * k0 * kaw_v
        dk0 = dk * (1.0 + (a - 1.0) * kaw_v)
        dkaw_ref[...] += _colsum(dk * k0 * (a - 1.0))
        inv = 1.0 / f["nrm"]
        proj = _head_sum(dkk * kk, bd)
        dkk0 = jnp.where(f["ss"] > 1e-24, (dkk - kk * proj) * inv, dkk * inv)
        dk0 = dk0 + dkk0 * kkw_v
        dkkw_ref[...] += _colsum(dkk0 * k0)
        dza = da * a * (1.0 - a)
        da0_ref[...] += _colsum(dza)
        dz = -dws_ref[...] * f["dec"] * f["e"] * (1.0 - f["sz"])
        dw0_ref[...] += _colsum(dz)
        dll = jnp.concatenate([dz, dza], axis=1).astype(BF16)
        dwl_ref[...] += _dot_tn(f["lin"].astype(BF16), dll)
        dlin = _dot_nt(dll, wl_v)
        lane = lax.broadcasted_iota(jnp.int32, (1, LANES), 1)
        th = f["th"]
        dlo = jnp.where(lane < LORA, dlin * (1.0 - th * th), dlin)
        dus = jnp.concatenate([drs_ref[...] + drb_ref[...], dk0, dvs_ref[...] + dvb_ref[...], dlo, dgt_ref[...]],
                              axis=1)
        dmu_ref[...] += _colsum(dus * f["delta"])
        g1 = dus * mu_v
        rows = lax.broadcasted_iota(jnp.int32, (tm, 1), 0)
        up = jnp.where(rows == tm - 1, carry[...], pltpu.roll(g1, tm - 1, 0))
        dua = dus - g1 + up
        du_ref[...] = dua
        dwa_ref[...] += jnp.dot(ht_ref[...], dua.astype(BF16), preferred_element_type=F32)
        carry[...] = g1[0:1, :]

    rev = lambda w: pl.BlockSpec((tm, w), lambda i: (nb - 1 - i, 0))
    vec = lambda w: pl.BlockSpec((1, w), lambda i: (0, 0))
    wl_spec = pl.BlockSpec((LANES, 2 * D_HALF), lambda i: (0, 0))
    return pl.pallas_call(
        body, name="rwkv_prep_bwd", grid=(nb,),
        in_specs=[rev(SEC), pl.BlockSpec((8, SEC), lambda i: (jnp.maximum((nb - 1 - i) * (tm // 8) - 1, 0), 0)),
                  pl.BlockSpec((d, tm), lambda i: (0, nb - 1 - i))]
                 + [rev(D_HALF)] * 10 + [vec(SEC), wl_spec] + [vec(D_HALF)] * 4,
        out_specs=[rev(SEC), pl.BlockSpec((d, SEC), lambda i: (0, 0)), vec(SEC), wl_spec] + [vec(D_HALF)] * 4,
        out_shape=[jax.ShapeDtypeStruct((s, SEC), F32), jax.ShapeDtypeStruct((d, SEC), F32),
                   jax.ShapeDtypeStruct((1, SEC), F32),
                   jax.ShapeDtypeStruct((LANES, 2 * D_HALF), F32)] + [jax.ShapeDtypeStruct((1, D_HALF), F32)] * 4,
        scratch_shapes=[pltpu.VMEM((1, SEC), F32)],
        compiler_params=_params("arbitrary"),
    )(u_a, u_a, h_t, *grads, mu, wl, w0, a0, kkw, kaw)


def _tri(tm, lower):
    r = lax.broadcasted_iota(jnp.int32, (tm, tm), 0)
    c = lax.broadcasted_iota(jnp.int32, (tm, tm), 1)
    return ((r >= c) if lower else (r <= c)).astype(BF16)


def _head_rms(x, g, bd):
    rinv = lax.rsqrt(_head_sum(x * x, bd) * (1.0 / HEAD) + RMS_EPS)
    xh = x * rinv
    return xh, rinv, xh * g


def _fox_front(h, w_b, fb, qg, kg, tm=256):
    s, d = h.shape

    def body(h_ref, wb_ref, fb_ref, qg_ref, kg_ref, ub_ref, q_ref, k_ref, v_ref, cc_ref, cr_ref, carry):
        i = pl.program_id(0)

        @pl.when(i == 0)
        def _():
            carry[...] = jnp.zeros_like(carry)

        ub_ref[...] = jnp.dot(h_ref[...], wb_ref[...], preferred_element_type=F32)
        bd = _head_ones()
        _, _, qn = _head_rms(ub_ref[:, 0:512], qg_ref[...], bd)
        _, _, kn = _head_rms(ub_ref[:, 512:1024], kg_ref[...], bd)
        q_ref[...] = (qn * ATT_SCALE).astype(BF16)
        k_ref[...] = kn.astype(BF16)
        v_ref[...] = ub_ref[:, 1024:1536].astype(BF16)
        lane = lax.broadcasted_iota(jnp.int32, (1, LANES), 1)
        logf = jnp.where(lane < N_HEADS, _log_sigmoid(ub_ref[:, 2048:2176] + fb_ref[...]), 0.0)
        cum = _exact_dot(logf, _tri(tm, True), ones_first=True) + carry[...]
        for h in range(N_HEADS):
            cc_ref[h] = jnp.broadcast_to(cum[:, h:h + 1], (tm, LANES))
        cr_ref[...] = jnp.transpose(cum)[0:N_HEADS, :]
        carry[...] = cum[tm - 1:tm, :]

    blk = pl.BlockSpec((tm, D_HALF), lambda i: (i, 0))
    return pl.pallas_call(
        body, name="fox_front", grid=(s // tm,),
        in_specs=[pl.BlockSpec((tm, d), lambda i: (i, 0)),
                  pl.BlockSpec(w_b.shape, lambda i: (0, 0), pipeline_mode=pl.Buffered(1)),
                  pl.BlockSpec((1, LANES), lambda i: (0, 0)),
                  pl.BlockSpec((1, D_HALF), lambda i: (0, 0)), pl.BlockSpec((1, D_HALF), lambda i: (0, 0))],
        out_specs=[pl.BlockSpec((tm, SEC), lambda i: (i, 0)), blk, blk, blk,
                   pl.BlockSpec((N_HEADS, tm, LANES), lambda i: (0, i, 0)), pl.BlockSpec((N_HEADS, tm), lambda i: (0, i))],
        out_shape=[jax.ShapeDtypeStruct((s, SEC), F32)] + [jax.ShapeDtypeStruct((s, D_HALF), BF16)] * 3
                  + [jax.ShapeDtypeStruct((N_HEADS, s, LANES), F32), jax.ShapeDtypeStruct((N_HEADS, s), F32)],
        scratch_shapes=[pltpu.VMEM((1, LANES), F32)],
        compiler_params=_params("arbitrary"),
    )(h, w_b, fb, qg, kg)


ATT_T = 256


def _tiles(nblk, by_query):
    if by_query:
        pairs = [(i, j) for i in range(nblk) for j in range(i + 1)]
    else:
        pairs = [(i, j) for j in range(nblk) for i in range(j, nblk)]
    return (jnp.asarray([p[0] for p in pairs], jnp.int32), jnp.asarray([p[1] for p in pairs], jnp.int32))


def _attn_fwd(q, k, v, cc, cr):
    s = q.shape[0]
    t = ATT_T
    nblk = s // t

    def body(qi_ref, kj_ref, q_ref, k_ref, v_ref, cc_ref, cr_ref, o_ref, lse_ref, m_sc, l_sc, acc_sc):
        i = qi_ref[pl.program_id(0)]
        j = kj_ref[pl.program_id(0)]

        @pl.when(j == 0)
        def _():
            m_sc[...] = jnp.full_like(m_sc, NEG)
            l_sc[...] = jnp.zeros_like(l_sc)
            acc_sc[...] = jnp.zeros_like(acc_sc)

        def tile(on_diagonal):
            causal = _causal_tile(t) if on_diagonal else None
            left = lax.broadcasted_iota(jnp.int32, (1, LANES), 1) < HEAD
            for p in range(N_PAIRS):
                lanes = slice(p * LANES, (p + 1) * LANES)
                q2, k2, v2 = q_ref[:, lanes], k_ref[:, lanes], v_ref[:, lanes]
                acc2 = acc_sc[:, lanes]
                for e in range(2):
                    h = 2 * p + e
                    msk = left if e == 0 else jnp.logical_not(left)
                    sc = _dot_nt(jnp.where(msk, q2, jnp.zeros_like(q2)), k2)
                    sc = sc + (_wide(cc_ref[h]) - cr_ref[h:h + 1, :])
                    if on_diagonal:
                        sc = jnp.where(causal, sc, NEG)
                    m_prev = m_sc[h]
                    m_new = jnp.maximum(m_prev, jnp.max(sc, axis=1, keepdims=True))
                    alpha = jnp.exp(m_prev - m_new)
                    pm = jnp.exp(sc - _wide(m_new))
                    l_sc[h] = alpha * l_sc[h] + jnp.sum(pm, axis=1, keepdims=True)
                    m_sc[h] = m_new
                    pv = jnp.dot(pm.astype(BF16), v2, preferred_element_type=F32)
                    acc2 = jnp.where(msk, alpha * acc2 + pv, acc2)
                acc_sc[:, lanes] = acc2

        pl.when(j < i)(functools.partial(tile, False))
        pl.when(j == i)(functools.partial(tile, True))

        @pl.when(j == i)
        def _():
            left = lax.broadcasted_iota(jnp.int32, (1, LANES), 1) < HEAD
            for p in range(N_PAIRS):
                lanes = slice(p * LANES, (p + 1) * LANES)
                inv = jnp.where(left, 1.0 / l_sc[2 * p], 1.0 / l_sc[2 * p + 1])
                o_ref[:, lanes] = acc_sc[:, lanes] * inv
            for h in range(N_HEADS):
                lse_ref[h] = m_sc[h] + jnp.log(l_sc[h])

    qi, kj = _tiles(nblk, by_query=True)
    qblk = pl.BlockSpec((t, D_HALF), lambda n, qi, kj: (qi[n], 0))
    kblk = pl.BlockSpec((t, D_HALF), lambda n, qi, kj: (kj[n], 0))
    qrep = pl.BlockSpec((N_HEADS, t, LANES), lambda n, qi, kj: (0, qi[n], 0))
    return pl.pallas_call(
        body, name="fox_attn_fwd",
        grid_spec=pltpu.PrefetchScalarGridSpec(
            num_scalar_prefetch=2, grid=(qi.shape[0],),
            in_specs=[qblk, kblk, kblk, qrep, pl.BlockSpec((N_HEADS, t), lambda n, qi, kj: (0, kj[n]))],
            out_specs=[qblk, qrep],
            scratch_shapes=[pltpu.VMEM((N_HEADS, t, LANES), F32), pltpu.VMEM((N_HEADS, t, LANES), F32),
                            pltpu.VMEM((t, D_HALF), F32)]),
        out_shape=[jax.ShapeDtypeStruct((s, D_HALF), F32), jax.ShapeDtypeStruct((N_HEADS, s, LANES), F32)],
        compiler_params=_params("arbitrary"),
    )(qi, kj, q, k, v, cc, cr)


def _causal_tile(t):
    return lax.broadcasted_iota(jnp.int32, (t, t), 0) >= lax.broadcasted_iota(jnp.int32, (t, t), 1)


def _wide(x):
    return jnp.concatenate([x, x], axis=1)


def _attn_probs(q2, k2, v2, do2, msk, causal, bias, lse_rows):
    zero = jnp.zeros_like(q2)
    qh = jnp.where(msk, q2, zero)
    doh = jnp.where(msk, do2, zero)
    sc = _dot_nt(qh, k2) + bias
    if causal is not None:
        sc = jnp.where(causal, sc, NEG)
    pm = jnp.exp(sc - _wide(lse_rows))
    dp = _dot_nt(doh, v2)
    return qh, doh, pm, dp


def _attn_bwd_rowdot(q, k, v, do, lse, cc, cr):
    s = q.shape[0]
    t = ATT_T
    nblk = s // t

    def body(qi_ref, kj_ref, q_ref, k_ref, v_ref, do_ref, lse_ref, cc_ref, cr_ref, dd_ref, acc):
        i = qi_ref[pl.program_id(0)]
        j = kj_ref[pl.program_id(0)]

        @pl.when(j == 0)
        def _():
            acc[...] = jnp.zeros_like(acc)

        def tile(on_diagonal):
            causal = _causal_tile(t) if on_diagonal else None
            left = lax.broadcasted_iota(jnp.int32, (1, LANES), 1) < HEAD
            for p in range(N_PAIRS):
                lanes = slice(p * LANES, (p + 1) * LANES)
                q2, k2, v2, do2 = q_ref[:, lanes], k_ref[:, lanes], v_ref[:, lanes], do_ref[:, lanes]
                for e in range(2):
                    h = 2 * p + e
                    msk = left if e == 0 else jnp.logical_not(left)
                    bias = _wide(cc_ref[h]) - cr_ref[h:h + 1, :]
                    _, _, pm, dp = _attn_probs(q2, k2, v2, do2, msk, causal, bias, lse_ref[h])
                    acc[h] += jnp.sum(pm * dp, axis=1, keepdims=True)

        pl.when(j < i)(functools.partial(tile, False))
        pl.when(j == i)(functools.partial(tile, True))

        @pl.when(j == i)
        def _():
            dd_ref[...] = acc[...]

    qi, kj = _tiles(nblk, by_query=True)
    qblk = pl.BlockSpec((t, D_HALF), lambda n, qi, kj: (qi[n], 0))
    qcol = pl.BlockSpec((N_HEADS, t, LANES), lambda n, qi, kj: (0, qi[n], 0))
    kblk = pl.BlockSpec((t, D_HALF), lambda n, qi, kj: (kj[n], 0))
    return pl.pallas_call(
        body, name="fox_attn_rowdot",
        grid_spec=pltpu.PrefetchScalarGridSpec(
            num_scalar_prefetch=2, grid=(qi.shape[0],),
            in_specs=[qblk, kblk, kblk, qblk, qcol, qcol, pl.BlockSpec((N_HEADS, t), lambda n, qi, kj: (0, kj[n]))],
            out_specs=qcol, scratch_shapes=[pltpu.VMEM((N_HEADS, t, LANES), F32)]),
        out_shape=jax.ShapeDtypeStruct((N_HEADS, s, LANES), F32),
        compiler_params=_params("arbitrary"),
    )(qi, kj, q, k, v, do, lse, cc, cr)


def _attn_bwd(q, k, v, do, lse, dd, cc, cr):
    s = q.shape[0]
    t = ATT_T
    nblk = s // t

    def body(qi_ref, kj_ref, q_ref, k_ref, v_ref, do_ref, lse_ref, dd_ref, cc_ref, cr_ref,
             dq_ref, dk_ref, dv_ref, dcr_ref, dk_sc, dv_sc, dcr_sc):
        i = qi_ref[pl.program_id(0)]
        j = kj_ref[pl.program_id(0)]

        @pl.when(pl.program_id(0) == 0)
        def _():
            dq_ref[...] = jnp.zeros_like(dq_ref)

        @pl.when(i == j)
        def _():
            dk_sc[...] = jnp.zeros_like(dk_sc)
            dv_sc[...] = jnp.zeros_like(dv_sc)
            dcr_sc[...] = jnp.zeros_like(dcr_sc)

        def tile(on_diagonal):
            causal = _causal_tile(t) if on_diagonal else None
            left = lax.broadcasted_iota(jnp.int32, (1, LANES), 1) < HEAD
            qrows = pl.ds(pl.multiple_of(i * t, t), t)
            for p in range(N_PAIRS):
                lanes = slice(p * LANES, (p + 1) * LANES)
                q2, k2, v2, do2 = q_ref[:, lanes], k_ref[:, lanes], v_ref[:, lanes], do_ref[:, lanes]
                zero = jnp.zeros_like(q2)
                dq2 = jnp.zeros((t, LANES), F32)
                dk2 = jnp.zeros((t, LANES), F32)
                dv2 = jnp.zeros((t, LANES), F32)
                for e in range(2):
                    h = 2 * p + e
                    msk = left if e == 0 else jnp.logical_not(left)
                    bias = _wide(cc_ref[h]) - cr_ref[h:h + 1, :]
                    qh, doh, pm, dp = _attn_probs(q2, k2, v2, do2, msk, causal, bias, lse_ref[h])
                    dsc = pm * (dp - _wide(dd_ref[h]))
                    dsb = dsc.astype(BF16)
                    dv2 += _dot_tn(pm.astype(BF16), doh)
                    dk2 += _dot_tn(dsb, qh)
                    dq2 += jnp.dot(dsb, jnp.where(msk, k2, zero), preferred_element_type=F32)
                    dcr_sc[h:h + 1, :] += -_colsum(dsc)
                dq_ref[qrows, lanes] += dq2 * ATT_SCALE
                dk_sc[:, lanes] += dk2
                dv_sc[:, lanes] += dv2

        pl.when(i > j)(functools.partial(tile, False))
        pl.when(i == j)(functools.partial(tile, True))

        @pl.when(i == nblk - 1)
        def _():
            dk_ref[...] = dk_sc[...]
            dv_ref[...] = dv_sc[...]
            dcr_ref[...] = dcr_sc[...]

    qi, kj = _tiles(nblk, by_query=False)
    qblk = pl.BlockSpec((t, D_HALF), lambda n, qi, kj: (qi[n], 0))
    qcol = pl.BlockSpec((N_HEADS, t, LANES), lambda n, qi, kj: (0, qi[n], 0))
    kblk = pl.BlockSpec((t, D_HALF), lambda n, qi, kj: (kj[n], 0))
    krow = pl.BlockSpec((N_HEADS, t), lambda n, qi, kj: (0, kj[n]))
    return pl.pallas_call(
        body, name="fox_attn_bwd",
        grid_spec=pltpu.PrefetchScalarGridSpec(
            num_scalar_prefetch=2, grid=(qi.shape[0],),
            in_specs=[qblk, kblk, kblk, qblk, qcol, qcol, qcol, krow],
            out_specs=[pl.BlockSpec((s, D_HALF), lambda n, qi, kj: (0, 0)), kblk, kblk, krow],
            scratch_shapes=[pltpu.VMEM((t, D_HALF), F32), pltpu.VMEM((t, D_HALF), F32), pltpu.VMEM((N_HEADS, t), F32)]),
        out_shape=[jax.ShapeDtypeStruct((s, D_HALF), F32)] * 3 + [jax.ShapeDtypeStruct((N_HEADS, s), F32)],
        compiler_params=_params("arbitrary"),
    )(qi, kj, q, k, v, do, lse, dd, cc, cr)


def _fox_prep_bwd(u_b, h_t, dq, dk, dv, dgate, dcum, fb, qg, kg, tm=256):
    s = u_b.shape[0]
    nb = s // tm
    d = h_t.shape[0]

    def body(ub_ref, ht_ref, dq_ref, dk_ref, dv_ref, dg_ref, dc_ref, fb_ref, qg_ref, kg_ref,
             du_ref, dwb_ref, dqg_ref, dkg_ref, dfb_ref, carry):
        i = pl.program_id(0)

        @pl.when(i == 0)
        def _():
            carry[...] = jnp.zeros_like(carry)
            dwb_ref[...] = jnp.zeros_like(dwb_ref)
            dqg_ref[...] = jnp.zeros_like(dqg_ref)
            dkg_ref[...] = jnp.zeros_like(dkg_ref)
            dfb_ref[...] = jnp.zeros_like(dfb_ref)

        bd = _head_ones()
        for lo, g_ref, d_ref, dgain_ref in ((0, qg_ref, dq_ref, dqg_ref), (512, kg_ref, dk_ref, dkg_ref)):
            gain = g_ref[...]
            xh, rinv, _ = _head_rms(ub_ref[:, lo:lo + 512], gain, bd)
            dn = d_ref[...]
            dgain_ref[...] += _colsum(dn * xh)
            dxh = dn * gain
            du_ref[:, lo:lo + 512] = rinv * (dxh - xh * (_head_sum(dxh * xh, bd) * (1.0 / HEAD)))
        du_ref[:, 1024:1536] = dv_ref[...]
        du_ref[:, 1536:2048] = dg_ref[...]
        lane = lax.broadcasted_iota(jnp.int32, (1, LANES), 1)
        dc = dc_ref[...]
        dlogf = _exact_dot(dc, _tri(tm, False), ones_first=True) + carry[...]
        carry[...] += _colsum(dc)
        fl = ub_ref[:, 2048:2176] + fb_ref[...]
        dfl = jnp.where(lane < N_HEADS, dlogf * (1.0 - _sigmoid(fl)), 0.0)
        du_ref[:, 2048:2176] = dfl
        dfb_ref[...] += _colsum(dfl)
        dwb_ref[...] += jnp.dot(ht_ref[...], du_ref[...].astype(BF16), preferred_element_type=F32)

    rev = lambda w: pl.BlockSpec((tm, w), lambda i: (nb - 1 - i, 0))
    vec = lambda w: pl.BlockSpec((1, w), lambda i: (0, 0))
    return pl.pallas_call(
        body, name="fox_prep_bwd", grid=(nb,),
        in_specs=[rev(SEC), pl.BlockSpec((d, tm), lambda i: (0, nb - 1 - i))] + [rev(D_HALF)] * 4
                 + [rev(LANES), vec(LANES), vec(D_HALF), vec(D_HALF)],
        out_specs=[rev(SEC), pl.BlockSpec((d, SEC), lambda i: (0, 0)), vec(D_HALF), vec(D_HALF), vec(LANES)],
        out_shape=[jax.ShapeDtypeStruct((s, SEC), F32), jax.ShapeDtypeStruct((d, SEC), F32),
                   jax.ShapeDtypeStruct((1, D_HALF), F32), jax.ShapeDtypeStruct((1, D_HALF), F32),
                   jax.ShapeDtypeStruct((1, LANES), F32)],
        scratch_shapes=[pltpu.VMEM((1, LANES), F32)],
        compiler_params=_params("arbitrary"),
    )(u_b, h_t, dq, dk, dv, dgate, dcum, fb, qg, kg)


def _merge(y, r, k, v, gate_a, o, u_b, h, x, tgt, w_g, wa, wb, wo, fg, lw, lb, rk, tm=256):
    s, d = x.shape

    def body(y_ref, r_ref, k_ref, v_ref, ga_ref, o_ref, gb_ref, h_ref, x_ref, t_ref, wg_ref, wa_ref, wb_ref, wo_ref,
             fg_ref, lw_ref, lb_ref, rk_ref,
             dx2_ref, dy_ref, drb_ref, dkb_ref, dvb_ref, dga_ref, do_ref, dgb_ref, dug_ref,
             dwa_ref, dwb_ref, dwo_ref, dfg_ref, loss_ref, dlw_ref, dlb_ref, drk_ref):
        i = pl.program_id(0)

        @pl.when(i == 0)
        def _():
            for ref in (dwa_ref, dwb_ref, dwo_ref, dfg_ref, loss_ref, dlw_ref, dlb_ref, drk_ref):
                ref[...] = jnp.zeros_like(ref)

        bd = _head_ones()
        wa_v, wb_v, wo_v, fg_v = wa_ref[...], wb_ref[...], wo_ref[...], fg_ref[...]
        rv, kv, vv, ga, lw_v, rk_v = r_ref[...], k_ref[...], v_ref[...], ga_ref[...], lw_ref[...], rk_ref[...]
        yn, rstd, rkk, sga, pre = _rwkv_post_math(y_ref[...], rv, kv, vv, ga, lw_v, lb_ref[...], rk_v, bd)
        silu_a = ga * sga
        gb, ov = gb_ref[...], o_ref[...]
        sgb = _sigmoid(gb)
        silu_b = gb * sgb
        ma = (pre * silu_a).astype(BF16)
        mb = (ov * silu_b).astype(BF16)
        ya = jnp.dot(ma, wa_v, preferred_element_type=F32)
        yb = jnp.dot(mb, wb_v, preferred_element_type=F32)
        ug = jnp.dot(h_ref[...], wg_ref[...], preferred_element_type=F32)
        sa = _sigmoid(ug[:, 0:d])
        sb = _sigmoid(ug[:, d:2 * d])
        merged = (sa * ya + sb * yb).astype(BF16)
        x2 = x_ref[...] + jnp.dot(merged, wo_v, preferred_element_type=F32)
        r2 = lax.rsqrt(jnp.mean(x2 * x2, axis=-1, keepdims=True) + RMS_EPS)
        x2h = x2 * r2
        err = x2h * fg_v - t_ref[...]
        loss_ref[...] += _colsum(err * err)
        dyo = err * (1.0 / d)
        dfg_ref[...] += _colsum(dyo * x2h)
        dx2h = dyo * fg_v
        dx2 = r2 * (dx2h - x2h * jnp.mean(dx2h * x2h, axis=-1, keepdims=True))
        dx2_ref[...] = dx2
        dx2b = dx2.astype(BF16)
        dmerged = _dot_nt(dx2b, wo_v)
        dwo_ref[...] += _dot_tn(merged, dx2b)
        dya = dmerged * sa
        dyb = dmerged * sb
        dug_ref[:, 0:d] = dya * ya * (1.0 - sa)
        dug_ref[:, d:2 * d] = dyb * yb * (1.0 - sb)
        dyab = dya.astype(BF16)
        dybb = dyb.astype(BF16)
        dwa_ref[...] += _dot_tn(ma, dyab)
        dwb_ref[...] += _dot_tn(mb, dybb)
        dmb = _dot_nt(dybb, wb_v)
        do_ref[...] = (dmb * silu_b).astype(BF16)
        dgb_ref[...] = dmb * ov * (sgb * (1.0 + gb * (1.0 - sgb)))
        dma = _dot_nt(dyab, wa_v)
        dga_ref[...] = dma * pre * (sga * (1.0 + ga * (1.0 - sga)))
        dpre = dma * silu_a
        dlw_ref[...] += _colsum(dpre * yn)
        dlb_ref[...] += _colsum(dpre)
        dyn = dpre * lw_v
        m1 = _head_sum(dyn, bd) * (1.0 / HEAD)
        m2 = _head_sum(dyn * yn, bd) * (1.0 / HEAD)
        dy_ref[...] = rstd * (dyn - m1 - yn * m2)
        dvb_ref[...] = dpre * rkk
        drkk = _head_sum(dpre * vv, bd)
        drb_ref[...] = drkk * kv * rk_v
        dkb_ref[...] = drkk * rv * rk_v
        drk_ref[...] += _colsum(drkk * rv * kv)

    row = lambda w: pl.BlockSpec((tm, w), lambda i: (i, 0))
    full = lambda a: pl.BlockSpec(a.shape, lambda i: (0, 0))
    once = lambda a: pl.BlockSpec(a.shape, lambda i: (0, 0), pipeline_mode=pl.Buffered(1))
    half = jax.ShapeDtypeStruct((s, D_HALF), F32)
    fshape = lambda a: jax.ShapeDtypeStruct(a.shape, F32)
    return pl.pallas_call(
        body, name="merge_fwd_bwd", grid=(s // tm,),
        in_specs=[row(D_HALF)] * 6 + [pl.BlockSpec((tm, D_HALF), lambda i: (i, 3)), row(d), row(d), row(d),
                                      once(w_g), once(wa), once(wb), once(wo), full(fg), full(lw), full(lb), full(rk)],
        out_specs=[row(d)] + [row(D_HALF)] * 7 + [row(GATE_COLS), full(wa), full(wb), full(wo), full(fg), full(fg),
                                                   full(lw), full(lb), full(rk)],
        out_shape=[jax.ShapeDtypeStruct((s, d), F32)] + [half] * 5 + [jax.ShapeDtypeStruct((s, D_HALF), BF16), half,
                                                                    jax.ShapeDtypeStruct((s, GATE_COLS), F32),
                                                                    fshape(wa), fshape(wb), fshape(wo), fshape(fg),
                                                                    fshape(fg), fshape(lw), fshape(lb), fshape(rk)],
        compiler_params=_params("arbitrary"),
    )(y, r, k, v, gate_a, o, u_b, h, x, tgt, w_g, wa, wb, wo, fg, lw, lb, rk)


def _lora_weight(w_up, a_up):
    z = jnp.zeros((LORA, D_HALF), w_up.dtype)
    return jnp.concatenate([jnp.concatenate([w_up, z], axis=1), jnp.concatenate([z, a_up], axis=1)], axis=0)


def _device_grads(x, tgt, p, w_a, w_up, a_up, late_weights, fwd_exchange=None, bwd_exchange=None, tail_exchange=None):
    wl = _lora_weight(w_up, a_up)
    rk = p["r_k"].reshape(1, D_HALF)
    fb = jnp.pad(p["f_bias"], ((0, 0), (0, LANES - N_HEADS)))
    qg = jnp.tile(p["q_norm_g"], (1, N_HEADS))
    kg = jnp.tile(p["k_norm_g"], (1, N_HEADS))
    fg = p["final_norm_g"].reshape(1, D_MODEL)
    mixer = (p["shift_mu"], wl, p["w0"], p["a0"], p["k_k"], p["k_a"])

    h, u_a, r, dec, k, v, av, bv, gate_a = _rwkv_front(x, p["norm_g"], w_a, *mixer)
    y, st, arrived = _wkv_fwd(r, dec, k, av, bv, v, fwd_exchange)

    w_b, w_g, w_out_a, w_out_b, w_out = late_weights(arrived)
    u_b, q, kn, vb, cc, cr = _fox_front(h, w_b, fb, qg, kg)
    o, lse = _attn_fwd(q, kn, vb, cc, cr)

    (dx2, dy, dr_b, dk_b, dv_b, dgate_a, do, dgate_b, du_g, dwa, dwb, dwo, dfg, loss_vec, dlw, dlb, drk) = _merge(
        y, r, k, v, gate_a, o, u_b, h, x, tgt, w_g, w_out_a, w_out_b, w_out, fg, p["lnx_w"], p["lnx_b"], rk)

    dd = _attn_bwd_rowdot(q, kn, vb, do, lse, cc, cr)
    dq, dk_att, dv_att, dcr = _attn_bwd(q, kn, vb, do, lse, dd, cc, cr)
    dcum = jnp.pad(dcr.T, ((0, 0), (0, LANES - N_HEADS)))
    h_t = h.T
    du_b, dw_b, dqg, dkg, dfb = _fox_prep_bwd(u_b, h_t, dq, dk_att, dv_att, dgate_b, dcum, fb, qg, kg)
    dw_g = _matmul_tn_acc(h_t, du_g, "dw_gate")

    scan_grads, sent = _wkv_bwd(r, dec, k, av, bv, v, dy, st,
                                bwd_exchange(dw_b, dw_g, dwa, dwb, dwo) if bwd_exchange else None)
    du_a, dw_a, dmu, dwl, dw0, da0, dkkw, dkaw = _rwkv_prep_bwd(
        u_a, h_t, (*scan_grads, dr_b, dk_b, dv_b, dgate_a), *mixer)
    dw_up, da_up = dwl[:LORA, :D_HALF], dwl[LORA:, D_HALF:]
    sent_last = _run_on_sequencer(tail_exchange(dw_a, dw_up, da_up), "scatter_tail", 1) if tail_exchange else []
    grad_x, dnorm_g, _ = _inproj_bwd(du_a, du_b, du_g, w_a, w_b, w_g, x, dx2, p["norm_g"])

    grads = dict(
        norm_g=dnorm_g, w_in=(dw_a, dw_b, dw_g), shift_mu=dmu,
        w_lora_up=dw_up, w0=dw0, a_lora_up=da_up, a0=da0, k_k=dkkw, k_a=dkaw,
        r_k=drk.reshape(1, N_HEADS, HEAD), lnx_w=dlw, lnx_b=dlb, f_bias=dfb[:, :N_HEADS],
        q_norm_g=dqg.reshape(N_HEADS, HEAD).sum(axis=0, keepdims=True),
        k_norm_g=dkg.reshape(N_HEADS, HEAD).sum(axis=0, keepdims=True),
        w_out_a=dwa, w_out_b=dwb, w_out=dwo, final_norm_g=dfg.reshape(D_MODEL))
    return loss_vec, grad_x, grads, sent, sent_last


CHIP_FLIPS = ((1, 0), (0, 1), (1, 1))
ANY = pl.BlockSpec(memory_space=pl.ANY)


def _position():
    return lax.axis_index("x"), lax.axis_index("y"), lax.axis_index("c")


def _flip(v, f):
    return 1 - v if f else v


def _both(a, b):
    if a is None:
        return b
    return a if b is None else jnp.logical_and(a, b)


def _when(cond, fn):
    if cond is None:
        fn()
    else:
        pl.when(cond)(fn)


class _Moves:
    def __init__(self, send_sems, recv_sems, local_sems):
        self.send_sems, self.recv_sems, self.local_sems = send_sems, recv_sems, local_sems
        self.remote, self.local = [], []

    def send(self, src, dst, peer, landing, send_if=None, recv_if=None, first=False):
        k = len(self.remote)
        sems = dict(send_sem=self.send_sems.at[k], recv_sem=self.recv_sems.at[k], device_id=peer, device_id_type=MESH)
        out = pltpu.make_async_remote_copy(src_ref=src, dst_ref=dst, **sems)
        arrival = pltpu.make_async_remote_copy(src_ref=src, dst_ref=landing, **sems)
        self.remote.append((out, arrival, send_if, recv_if, first))

    def copy(self, src, dst, cond=None):
        cp = pltpu.make_async_copy(src, dst, self.local_sems.at[len(self.local)])
        self.local.append((cp, cond))

    def start(self, also=None):
        for cp, cond in self.local:
            _when(_both(also, cond), cp.start)
        for out, _, send_if, _, _ in self.remote:
            _when(_both(also, send_if), out.start)

    def wait_arrivals(self, also=None, first=None):
        for _, arrival, _, recv_if, is_first in self.remote:
            if first is None or first == is_first:
                _when(_both(also, recv_if), arrival.wait_recv)

    def wait_sent(self, also=None):
        for out, _, send_if, _, _ in self.remote:
            _when(_both(also, send_if), out.wait_send)
        for cp, cond in self.local:
            _when(_both(also, cond), cp.wait)

    def wait(self, also=None):
        self.wait_arrivals(also)
        self.wait_sent(also)


class _Exchange:
    def __init__(self, operands, out_shapes, n_remote, n_local, build, relays=None, in_place=(), n_staging=0):
        self.operands, self.out_shapes = list(operands), list(out_shapes)
        self.n_remote, self.n_local, self.build = n_remote, n_local, build
        self.relays, self.in_place = relays, in_place
        self.n_staging = n_staging

    def scratch(self):
        return [pltpu.SemaphoreType.DMA((self.n_remote,)), pltpu.SemaphoreType.DMA((self.n_remote,)),
                pltpu.SemaphoreType.DMA((max(self.n_local, 1),))]

    def moves(self, in_refs, out_refs, sems):
        mv = _Moves(*sems)
        self.build(mv, in_refs, out_refs)
        return mv


def _run_on_sequencer(exchange, name, collective_id):
    ins = [jax.new_ref(a, memory_space=pltpu.MemorySpace.HBM) for a in exchange.operands]
    outs = [ins[i] if i in exchange.in_place else jax.empty_ref(s, memory_space=pltpu.MemorySpace.HBM)
            for i, s in enumerate(exchange.out_shapes)]
    forward, to_sibling = exchange.relays or (None, None)
    relay_scratch = [pltpu.SemaphoreType.DMA((stage[0],)) for stage in (forward, to_sibling) if stage for _ in range(2)]

    def launch(*sems):
        x, y, c = _position()
        peers = [(_flip(x, fx), _flip(y, fy), c) for fx, fy in CHIP_FLIPS] + ([(x, y, 1 - c)] if to_sibling else [])
        barrier = pltpu.get_barrier_semaphore()
        for peer in peers:
            pl.semaphore_signal(barrier, inc=1, device_id=peer, device_id_type=MESH)
        pl.semaphore_wait(barrier, len(peers))
        moves = exchange.moves(ins, outs, sems[:3])
        moves.start()
        later = []
        if forward:
            onward = _Moves(sems[3], sems[4], None)
            forward[1](onward, ins, outs)
            moves.wait_arrivals(first=True)
            onward.start()
            moves.wait_arrivals(first=False)
            onward.wait_arrivals()
            later.append(onward)
        else:
            moves.wait_arrivals()
        if to_sibling:
            passed = _Moves(*sems[-2:], None)
            to_sibling[1](passed, ins, outs)
            passed.start()
            passed.wait_arrivals()
            later.append(passed)
        for mv in later + [moves]:
            mv.wait_sent()

    pl.kernel(launch, mesh=plsc.ScalarSubcoreMesh(axis_name="sequencer", num_cores=1), name=name,
              scratch_types=tuple(exchange.scratch() + relay_scratch),
              compiler_params=pltpu.CompilerParams(collective_id=collective_id))()
    return [o[...] for o in outs[:len(outs) - exchange.n_staging]]


def _row_major_copy(a, name):
    r, c = a.shape
    tr = _row_tile(r)

    def body(a_ref, o_ref):
        o_ref[...] = a_ref[...]

    blk = pl.BlockSpec((tr, c), lambda i: (i, 0))
    return pl.pallas_call(body, name=name, grid=(r // tr,), in_specs=[blk], out_specs=blk,
                          out_shape=jax.ShapeDtypeStruct(a.shape, a.dtype), compiler_params=_params("parallel"))(a)


def _is_chip(x, y, chip):
    return jnp.logical_and(x == chip // 2, y == chip % 2)


def _gather_exchange(from_chip, from_all, split=()):
    n1, n2 = len(from_chip), len(from_all)
    near = CHIP_FLIPS[:2]

    def quarters(t, c, first, count=1):
        n = from_chip[t][1].shape[0] // 4
        return pl.ds((2 * c + first) * n, count * n)

    def build(mv, ins, outs):
        x, y, c = _position()
        me = 2 * x + y
        for t, (chip, _) in enumerate(from_chip):
            if t not in split:
                mv.copy(ins[t], outs[t], cond=_is_chip(x, y, chip))
        for t in range(n2):
            mv.copy(ins[n1 + t], outs[n1 + t].at[me])
        for t in split:
            for first in (True, False):
                for f, (fx, fy) in enumerate(near):
                    px, py = _flip(x, fx), _flip(y, fy)
                    part = quarters(t, c, f if first else 1 - f)
                    mv.send(ins[t].at[part], outs[t].at[part], (px, py, c), landing=outs[t].at[part], first=first,
                            send_if=_is_chip(x, y, from_chip[t][0]), recv_if=_is_chip(px, py, from_chip[t][0]))
        for fx, fy in CHIP_FLIPS:
            px, py = _flip(x, fx), _flip(y, fy)
            peer = (px, py, c)
            for t, (chip, _) in enumerate(from_chip):
                if t not in split:
                    mv.send(ins[t], outs[t], peer, landing=outs[t],
                            send_if=_is_chip(x, y, chip), recv_if=_is_chip(px, py, chip))
            for t in range(n2):
                mv.send(ins[n1 + t], outs[n1 + t].at[me], peer, landing=outs[n1 + t].at[2 * px + py])

    def forward(mv, ins, outs):
        x, y, c = _position()
        for t in split:
            chip = from_chip[t][0]
            for f, (fx, fy) in enumerate(near):
                gx, gy = near[1 - f]
                part = quarters(t, c, f)
                mv.send(outs[t].at[part], outs[t].at[part], (_flip(x, gx), _flip(y, gy), c), landing=outs[t].at[part],
                        send_if=_is_chip(_flip(x, fx), _flip(y, fy), chip), recv_if=_is_chip(1 - x, 1 - y, chip))

    def to_sibling(mv, ins, outs):
        x, y, c = _position()
        for t in split:
            came = jnp.logical_not(_is_chip(x, y, from_chip[t][0]))
            mv.send(outs[t].at[quarters(t, c, 0, 2)], outs[t].at[quarters(t, c, 0, 2)], (x, y, 1 - c),
                    landing=outs[t].at[quarters(t, 1 - c, 0, 2)], send_if=came, recv_if=came)

    arrays = [a for _, a in from_chip] + list(from_all)
    shapes = [jax.ShapeDtypeStruct(a.shape, a.dtype) for _, a in from_chip]
    shapes += [jax.ShapeDtypeStruct((N_CHIPS,) + a.shape, a.dtype) for a in from_all]
    n_remote = len(CHIP_FLIPS) * (n1 - len(split) + n2) + 2 * len(near) * len(split)
    relays = ((len(near) * len(split), forward), (len(split), to_sibling)) if split else None
    return _Exchange(arrays, shapes, n_remote, n1 + n2, build, relays, in_place=split)


def _scatter_exchange(to_chip, to_all, via_neighbours=False):
    n1, n2 = len(to_chip), len(to_all)
    near = CHIP_FLIPS[:2]
    direct = near if via_neighbours else CHIP_FLIPS

    def half(t, g):
        n = to_chip[t][1].shape[0] // 2
        return pl.ds(g * n, n)

    def build(mv, ins, outs):
        x, y, c = _position()
        if via_neighbours:
            for t, (chip, _) in enumerate(to_chip):
                for g, (gx, gy) in enumerate(near):
                    ox, oy = near[1 - g]
                    mv.send(ins[t].at[half(t, g)], outs[n1 + n2 + t], (_flip(x, gx), _flip(y, gy), c),
                            landing=outs[n1 + n2 + t], first=True, send_if=_is_chip(1 - x, 1 - y, chip),
                            recv_if=_is_chip(_flip(x, ox), _flip(y, oy), chip))
        for f, (fx, fy) in enumerate(CHIP_FLIPS):
            px, py = _flip(x, fx), _flip(y, fy)
            peer = (px, py, c)
            if (fx, fy) in direct:
                for t, (chip, _) in enumerate(to_chip):
                    mv.send(ins[t], outs[t].at[f], peer, landing=outs[t].at[f],
                            send_if=_is_chip(px, py, chip), recv_if=_is_chip(x, y, chip))
            for t in range(n2):
                mv.send(ins[n1 + t].at[2 * px + py], outs[n1 + t].at[f], peer, landing=outs[n1 + t].at[f])

    def forward(mv, ins, outs):
        x, y, c = _position()
        for t, (chip, _) in enumerate(to_chip):
            for g in range(len(near)):
                ox, oy = near[1 - g]
                far_slot = outs[t].at[len(near)].at[half(t, g)]
                mv.send(outs[n1 + n2 + t], far_slot, (_flip(x, ox), _flip(y, oy), c), landing=far_slot,
                        send_if=_is_chip(_flip(x, ox), _flip(y, oy), chip), recv_if=_is_chip(x, y, chip))

    arrays = [a for _, a in to_chip] + list(to_all)
    shapes = [jax.ShapeDtypeStruct((len(CHIP_FLIPS),) + a.shape, a.dtype) for _, a in to_chip]
    shapes += [jax.ShapeDtypeStruct((len(CHIP_FLIPS),) + a.shape[1:], a.dtype) for a in to_all]
    if not via_neighbours:
        return _Exchange(arrays, shapes, len(CHIP_FLIPS) * (n1 + n2), 0, build)
    shapes += [jax.ShapeDtypeStruct((a.shape[0] // 2, a.shape[1]), a.dtype) for _, a in to_chip]
    return _Exchange(arrays, shapes, 2 * len(near) * n1 + len(CHIP_FLIPS) * n2, 0, build,
                     relays=((len(near) * n1, forward), None), n_staging=n1)


def _swap_sibling(tensors, name):
    n = len(tensors)

    def body(*refs):
        ins, outs = refs[:n], refs[n:2 * n]
        send_sems, recv_sems = refs[2 * n:]
        x, y, c = _position()
        copies = [pltpu.make_async_remote_copy(
            src_ref=ins[t], dst_ref=outs[t], send_sem=send_sems.at[t], recv_sem=recv_sems.at[t],
            device_id=(x, y, 1 - c), device_id_type=MESH) for t in range(n)]
        for cp in copies:
            cp.start()
        for cp in copies:
            cp.wait_recv()
        for cp in copies:
            cp.wait_send()

    return pl.pallas_call(
        body, name=name, in_specs=[ANY] * n, out_specs=[ANY] * n,
        out_shape=[jax.ShapeDtypeStruct(a.shape, a.dtype) for a in tensors],
        scratch_shapes=[pltpu.SemaphoreType.DMA((n,)), pltpu.SemaphoreType.DMA((n,))],
        compiler_params=pltpu.CompilerParams(has_side_effects=True),
    )(*tensors)


def _pair_halves(g):
    r, cols = g.shape
    half = r // 2

    def body(g_ref, o_ref, mine, theirs, send_sem, recv_sem, local_sem):
        x, y, c = _position()
        away = pltpu.make_async_remote_copy(
            src_ref=g_ref.at[pl.ds((1 - c) * half, half)], dst_ref=theirs, send_sem=send_sem, recv_sem=recv_sem,
            device_id=(x, y, 1 - c), device_id_type=MESH)
        kept = pltpu.make_async_copy(g_ref.at[pl.ds(c * half, half)], mine, local_sem)
        away.start()
        kept.start()
        kept.wait()
        away.wait_recv()
        o_ref[...] = (mine[...] + theirs[...]).astype(BF16)
        away.wait_send()

    return pl.pallas_call(
        body, name="pair_halves", in_specs=[ANY], out_specs=pl.BlockSpec(memory_space=pltpu.VMEM),
        out_shape=jax.ShapeDtypeStruct((half, cols), BF16),
        scratch_shapes=[pltpu.VMEM((half, cols), F32), pltpu.VMEM((half, cols), F32),
                        pltpu.SemaphoreType.DMA(()), pltpu.SemaphoreType.DMA(()), pltpu.SemaphoreType.DMA(())],
        compiler_params=pltpu.CompilerParams(has_side_effects=True, vmem_limit_bytes=VMEM_LIMIT),
    )(g)


def _allreduce_small(slab):
    stages = 3

    def body(x_ref, o_ref, buf, send_sems, recv_sems):
        x, y, c = _position()
        peers = ((1 - x, y, c), (x, 1 - y, c), (x, y, 1 - c))
        o_ref[...] = x_ref[...]
        for k, peer in enumerate(peers):
            cp = pltpu.make_async_remote_copy(src_ref=o_ref, dst_ref=buf.at[k], send_sem=send_sems.at[k],
                                              recv_sem=recv_sems.at[k], device_id=peer, device_id_type=MESH)
            cp.start()
            cp.wait()
            o_ref[...] = o_ref[...] + buf[k]

    return pl.pallas_call(
        body, name="allreduce_small",
        in_specs=[pl.BlockSpec(memory_space=pltpu.VMEM)], out_specs=pl.BlockSpec(memory_space=pltpu.VMEM),
        out_shape=jax.ShapeDtypeStruct(slab.shape, slab.dtype),
        scratch_shapes=[pltpu.VMEM((stages,) + slab.shape, slab.dtype),
                        pltpu.SemaphoreType.DMA((stages,)), pltpu.SemaphoreType.DMA((stages,))],
        compiler_params=pltpu.CompilerParams(has_side_effects=True),
    )(slab)


def _row_tile(r):
    return min(r, 256)


def _sum4(stack, recv, me):
    _, r, c = stack.shape
    tr = _row_tile(r)

    def body(me_ref, own_ref, recv_ref, o_ref):
        o_ref[...] = (((own_ref[...] + recv_ref[0].astype(F32)) + recv_ref[1].astype(F32))
                      + recv_ref[2].astype(F32))

    return pl.pallas_call(
        body, name="sum_partials",
        grid_spec=pltpu.PrefetchScalarGridSpec(
            num_scalar_prefetch=1, grid=(r // tr,),
            in_specs=[pl.BlockSpec((None, tr, c), lambda i, me_ref: (me_ref[0], i, 0)),
                      pl.BlockSpec((len(CHIP_FLIPS), tr, c), lambda i, me_ref: (0, i, 0))],
            out_specs=pl.BlockSpec((tr, c), lambda i, me_ref: (i, 0))),
        out_shape=jax.ShapeDtypeStruct((r, c), F32), compiler_params=_params("parallel"),
    )(me, stack, recv)


def _sum_block(own, recv):
    r, c = recv.shape[1:]
    tr = _row_tile(r)

    def body(own_ref, recv_ref, o_ref):
        o_ref[...] = (((own_ref[:, :c] + recv_ref[0].astype(F32)) + recv_ref[1].astype(F32))
                      + recv_ref[2].astype(F32))

    return pl.pallas_call(
        body, name="sum_block", grid=(r // tr,),
        in_specs=[pl.BlockSpec((tr, own.shape[1]), lambda i: (i, 0)),
                  pl.BlockSpec((len(CHIP_FLIPS), tr, c), lambda i: (0, i, 0))],
        out_specs=pl.BlockSpec((tr, c), lambda i: (i, 0)),
        out_shape=jax.ShapeDtypeStruct((r, c), F32), compiler_params=_params("parallel"),
    )(own, recv)


def _sum_half(own, recv, core):
    r, c = own.shape[0], recv.shape[2]
    tr = _row_tile(r // 2)
    per_half = r // 2 // tr

    def body(core_ref, own_ref, recv_ref, o_ref):
        mine = pl.program_id(0) // per_half == core_ref[0]

        @pl.when(mine)
        def _():
            o_ref[...] = (((own_ref[:, :c] + recv_ref[0].astype(F32)) + recv_ref[1].astype(F32))
                          + recv_ref[2].astype(F32))

        @pl.when(jnp.logical_not(mine))
        def _():
            o_ref[...] = own_ref[:, :c]

    return pl.pallas_call(
        body, name="sum_half",
        grid_spec=pltpu.PrefetchScalarGridSpec(
            num_scalar_prefetch=1, grid=(r // tr,),
            in_specs=[pl.BlockSpec((tr, own.shape[1]), lambda i, core: (i, 0)),
                      pl.BlockSpec((len(CHIP_FLIPS), tr, c), lambda i, core: (0, i % per_half, 0))],
            out_specs=pl.BlockSpec((tr, c), lambda i, core: (i, 0))),
        out_shape=jax.ShapeDtypeStruct((r, c), F32), compiler_params=_params("parallel"),
    )(core, own, recv)


def _adamw_math(w, g, m, v):
    m = ADAM_B1 * m + (1.0 - ADAM_B1) * g
    v = ADAM_B2 * v + (1.0 - ADAM_B2) * (g * g)
    m_hat = m / (1.0 - ADAM_B1 ** ADAM_STEP)
    v_hat = v / (1.0 - ADAM_B2 ** ADAM_STEP)
    delta = -ADAM_LR * (m_hat / (jnp.sqrt(v_hat) + ADAM_EPS) + ADAM_WD * w)
    return delta, m, v


def _adamw(w, m, v, g_parts, name):
    r, c = w.shape
    tr = _row_tile(r)
    n = len(g_parts)

    def body(*refs):
        w_ref, m_ref, v_ref = refs[:3]
        g_refs = refs[3:3 + n]
        g_out, d_out, m_out, v_out, zero_out = refs[3 + n:]
        g = g_refs[0][...]
        for ref in g_refs[1:]:
            g = g + ref[...]
        g_out[...] = g
        d_out[...], m_out[...], v_out[...] = _adamw_math(w_ref[...], g, m_ref[...], v_ref[...])
        zero_out[0] = 0

    blk = pl.BlockSpec((tr, c), lambda i: (i, 0))
    return pl.pallas_call(
        body, name=name, grid=(r // tr,), in_specs=[blk] * (3 + n),
        out_specs=[blk] * 4 + [pl.BlockSpec(memory_space=pltpu.SMEM)],
        out_shape=[jax.ShapeDtypeStruct((r, c), F32)] * 4 + [jax.ShapeDtypeStruct((1,), jnp.int32)],
        compiler_params=_params("arbitrary"),
    )(w, m, v, *g_parts)


def _adamw_with_sibling(w, m, v, mine, name):
    r, c = w.shape
    tr = _row_tile(r)
    nb = r // tr

    def body(w_ref, m_ref, v_ref, mine_ref, mine_hbm, g_out, d_out, m_out, v_out, zero_out, theirs, send_sems, recv_sems):
        i = pl.program_id(0)
        x, y, core = _position()

        def tile(k):
            return pltpu.make_async_remote_copy(
                src_ref=mine_hbm.at[pl.ds(pl.multiple_of(k * tr, tr), tr)], dst_ref=theirs.at[k],
                send_sem=send_sems.at[k], recv_sem=recv_sems.at[k], device_id=(x, y, 1 - core), device_id_type=MESH)

        @pl.when(i == 0)
        def _():
            for k in range(nb):
                tile(k).start()

        tile(i).wait_recv()
        g = mine_ref[...] + theirs[i]
        g_out[...] = g
        d_out[...], m_out[...], v_out[...] = _adamw_math(w_ref[...], g, m_ref[...], v_ref[...])
        zero_out[0] = 0

        @pl.when(i == nb - 1)
        def _():
            for k in range(nb):
                tile(k).wait_send()

    blk = pl.BlockSpec((tr, c), lambda i: (i, 0))
    return pl.pallas_call(
        body, name=name, grid=(nb,), in_specs=[blk] * 4 + [ANY],
        out_specs=[blk] * 4 + [pl.BlockSpec(memory_space=pltpu.SMEM)],
        out_shape=[jax.ShapeDtypeStruct((r, c), F32)] * 4 + [jax.ShapeDtypeStruct((1,), jnp.int32)],
        scratch_shapes=[pltpu.VMEM((nb, tr, c), F32), pltpu.SemaphoreType.DMA((nb,)), pltpu.SemaphoreType.DMA((nb,))],
        compiler_params=pltpu.CompilerParams(dimension_semantics=("arbitrary",), has_side_effects=True,
                                             vmem_limit_bytes=VMEM_LIMIT),
    )(w, m, v, mine, mine)


def _adamw_small(total, w, m, v):
    sizes = [w[n].size for n in SMALL]
    flat = lambda d: [d[n].reshape(1, -1) for n in SMALL]
    k = len(SMALL)

    def body(*refs):
        total_ref, w_refs, m_refs, v_refs = refs[0], refs[1:1 + k], refs[1 + k:1 + 2 * k], refs[1 + 2 * k:1 + 3 * k]
        outs = refs[1 + 3 * k:]
        for i, size in enumerate(sizes):
            g = total_ref[i:i + 1, 0:size]
            outs[i][...] = g
            outs[k + i][...], outs[2 * k + i][...], outs[3 * k + i][...] = _adamw_math(
                w_refs[i][...], g, m_refs[i][...], v_refs[i][...])

    res = pl.pallas_call(
        body, name="adamw_small", out_shape=[jax.ShapeDtypeStruct((1, size), F32) for size in sizes] * 4,
        compiler_params=_params(),
    )(total, *flat(w), *flat(m), *flat(v))
    return [{n: res[j * k + i].reshape(w[n].shape) for i, n in enumerate(SMALL)} for j in range(4)]


SHARDED = ("w_in", "w_lora_up", "a_lora_up", "w_out_a", "w_out_b", "w_out")
ROW_SHARDED = ("w_out",)
SMALL = ("norm_g", "shift_mu", "w0", "a0", "k_k", "k_a", "r_k", "lnx_w", "lnx_b", "f_bias", "q_norm_g", "k_norm_g",
         "final_norm_g")
WEIGHTS = ("norm_g", "w_in", "shift_mu", "w_lora_up", "w0", "a_lora_up", "a0", "k_k", "k_a", "r_k", "lnx_w", "lnx_b",
           "f_bias", "q_norm_g", "k_norm_g", "w_out_a", "w_out_b", "w_out", "final_norm_g")
SLAB_ROWS = 16
SLAB_COLS = SEC


def _to_slab(named, extra=None):
    rows = [jnp.pad(named[n].reshape(1, -1), ((0, 0), (0, SLAB_COLS - named[n].size))) for n in SMALL]
    if extra is not None:
        rows.append(jnp.pad(extra.reshape(1, -1), ((0, 0), (0, SLAB_COLS - extra.size))))
    rows.append(jnp.zeros((SLAB_ROWS - len(rows), SLAB_COLS), F32))
    return jnp.concatenate(rows, axis=0)


def _by_chip(g, name):
    if name in ROW_SHARDED:
        return g.reshape(N_CHIPS, g.shape[0] // N_CHIPS, g.shape[1])
    r, c = g.shape
    return g.reshape(r, N_CHIPS, c // N_CHIPS).transpose(1, 0, 2)


def _from_chips(stack, name):
    if name in ROW_SHARDED:
        return stack.reshape(-1, stack.shape[2])
    _, r, c = stack.shape
    return stack.transpose(1, 0, 2).reshape(r, N_CHIPS * c)


def kernel(x, norm_g, w_in, shift_mu, w_lora_up, w0, a_lora_up, a0, k_k, k_a, r_k, lnx_w, lnx_b, f_bias, q_norm_g, k_norm_g, w_out_a, w_out_b, w_out, final_norm_g, loss_target, m_norm_g, m_w_in, m_shift_mu, m_w_lora_up, m_w0, m_a_lora_up, m_a0, m_k_k, m_k_a, m_r_k, m_lnx_w, m_lnx_b, m_f_bias, m_q_norm_g, m_k_norm_g, m_w_out_a, m_w_out_b, m_w_out, m_final_norm_g, v_norm_g, v_w_in, v_shift_mu, v_w_lora_up, v_w0, v_a_lora_up, v_a0, v_k_k, v_k_a, v_r_k, v_lnx_w, v_lnx_b, v_f_bias, v_q_norm_g, v_k_norm_g, v_w_out_a, v_w_out_b, v_w_out, v_final_norm_g):
    w = dict(norm_g=norm_g, w_in=w_in, shift_mu=shift_mu, w_lora_up=w_lora_up, w0=w0, a_lora_up=a_lora_up, a0=a0,
             k_k=k_k, k_a=k_a, r_k=r_k, lnx_w=lnx_w, lnx_b=lnx_b, f_bias=f_bias, q_norm_g=q_norm_g,
             k_norm_g=k_norm_g, w_out_a=w_out_a, w_out_b=w_out_b, w_out=w_out, final_norm_g=final_norm_g)
    m = dict(norm_g=m_norm_g, w_in=m_w_in, shift_mu=m_shift_mu, w_lora_up=m_w_lora_up, w0=m_w0,
             a_lora_up=m_a_lora_up, a0=m_a0, k_k=m_k_k, k_a=m_k_a, r_k=m_r_k, lnx_w=m_lnx_w, lnx_b=m_lnx_b,
             f_bias=m_f_bias, q_norm_g=m_q_norm_g, k_norm_g=m_k_norm_g, w_out_a=m_w_out_a, w_out_b=m_w_out_b,
             w_out=m_w_out, final_norm_g=m_final_norm_g)
    v = dict(norm_g=v_norm_g, w_in=v_w_in, shift_mu=v_shift_mu, w_lora_up=v_w_lora_up, w0=v_w0,
             a_lora_up=v_a_lora_up, a0=v_a0, k_k=v_k_k, k_a=v_k_a, r_k=v_r_k, lnx_w=v_lnx_w, lnx_b=v_lnx_b,
             f_bias=v_f_bias, q_norm_g=v_q_norm_g, k_norm_g=v_k_norm_g, w_out_a=v_w_out_a, w_out_b=v_w_out_b,
             w_out=v_w_out, final_norm_g=v_final_norm_g)
    shapes = {n: w[n].shape for n in WEIGHTS}

    shard = {n: w[n][0].astype(BF16) for n in SHARDED}
    late = ("w_out_a", "w_out_b", "w_out")
    loras = ("w_lora_up", "a_lora_up")
    w_in_head, w_in_tail = shard["w_in"][:, :A_TAIL], shard["w_in"][:, A_TAIL:]
    shard0, shard1_head, up_stack, aup_stack = _run_on_sequencer(_gather_exchange(
        [(0, shard["w_in"]), (1, w_in_head)], [shard[n] for n in loras], split=(0, 1)), "gather_early", 2)
    moments = (_row_major_copy(m["w_in"][0], "m_w_in_rows"), _row_major_copy(v["w_in"][0], "v_w_in_rows"))
    shard0, moments = lax.optimization_barrier((shard0, moments))
    w_a = jnp.concatenate([shard0, shard1_head], axis=1)

    def late_weights(arrived):
        shard1_tail, shard2, shard3 = arrived[:3]
        w_b = jnp.concatenate([shard1_tail, shard2[:, :B_TAIL], jnp.zeros((D_MODEL, SEC - FOX_REAL), BF16)], axis=1)
        w_g = jnp.concatenate([shard2[:, B_TAIL:], shard3], axis=1)
        return (w_b, w_g, *[_from_chips(s, n) for n, s in zip(late, arrived[3:])])

    own = {}
    cut = {"block0": lambda: own["dw_a"][:, :SHARD_COLS], "head1": lambda: own["dw_a"][:, SHARD_COLS:],
           "tail1": lambda: own["dw_b"][:, :B_HEAD],
           "block2": lambda: jnp.concatenate([own["dw_b"][:, B_HEAD:FOX_REAL], own["dw_g"][:, :G_HEAD]], axis=1),
           "block3": lambda: own["dw_g"][:, G_HEAD:]}

    def bwd_exchange(dw_b, dw_g, dwa, dwb, dwo):
        own.update(dw_b=dw_b, dw_g=dw_g)
        own.update({n: _by_chip(g, n) for n, g in zip(late, (dwa, dwb, dwo))})
        return _scatter_exchange([(1, cut["tail1"]().astype(BF16)), (2, cut["block2"]().astype(BF16)),
                                  (3, cut["block3"]().astype(BF16))], [own[n].astype(BF16) for n in late])

    def tail_exchange(dw_a, dw_up, da_up):
        own.update(dw_a=dw_a)
        own.update({n: _by_chip(g, n) for n, g in zip(loras, (dw_up, da_up))})
        pair = _pair_halves(dw_a)
        return _scatter_exchange([(0, pair[:, :SHARD_COLS]), (1, pair[:, SHARD_COLS:])],
                                 [own[n].astype(BF16) for n in loras], via_neighbours=True)

    small = {n: w[n] for n in SMALL}
    loss_vec, grad_x, grads, sent, sent_last = _device_grads(
        x[0], loss_target[0], small, w_a, _from_chips(up_stack, "w_lora_up"), _from_chips(aup_stack, "a_lora_up"),
        late_weights, _gather_exchange([(1, w_in_tail), (2, shard["w_in"]), (3, shard["w_in"])], [shard[n] for n in late]),
        bwd_exchange, tail_exchange)

    total = _allreduce_small(_to_slab(grads, extra=loss_vec))
    loss = (0.5 / D_MODEL) * jnp.sum(total[len(SMALL)])
    out_g, out_d, out_m, out_v = _adamw_small(total, w, m, v)

    xpos, ypos, cpos = _position()
    me = (2 * xpos + ypos).astype(jnp.int32).reshape(1)
    core = cpos.astype(jnp.int32).reshape(1)
    core_sum, theirs = {}, {}

    def update(n):
        if n == "w_in":
            g, d, m2, v2, zero = _adamw_with_sibling(w[n][0], *moments, core_sum[n], "adamw_" + n)
        else:
            g, d, m2, v2, zero = _adamw(w[n][0], m[n][0], v[n][0], [core_sum[n], theirs[n]], "adamw_" + n)
        out_g[n], out_d[n], out_m[n], out_v[n] = (a.reshape(shapes[n]) for a in (g, d, m2, v2))
        return zero

    sent_last, out_d["norm_g"] = lax.optimization_barrier((sent_last, out_d["norm_g"]))
    core_sum["w_in"] = lax.switch(me[0], [
        lambda: _sum_half(own["dw_a"], sent_last[0], core),
        lambda: jnp.concatenate([_sum_half(cut["head1"](), sent_last[1], core),
                                 _sum_block(own["dw_b"], sent[0])], axis=1),
        lambda: _sum_block(cut["block2"](), sent[1]),
        lambda: _sum_block(cut["block3"](), sent[2])])
    after_w_in = me + update("w_in")
    others = loras + late
    core_sum.update({n: _sum4(own[n], r, after_w_in) for n, r in zip(others, sent_last[2:] + sent[3:])})
    theirs.update(zip(others, _swap_sibling([core_sum[n] for n in others], "swap_sibling_late")))
    for n in others:
        update(n)

    return (loss, grad_x.reshape(x.shape), *[out_g[n] for n in WEIGHTS], *[out_d[n] for n in WEIGHTS],
            *[out_m[n] for n in WEIGHTS], *[out_v[n] for n in WEIGHTS])
```
